```python
import jax, jax.numpy as jnp
from jax import lax
import numpy as np

D_MODEL = 1024
BATCH = 8
SEQ = 2048
DEPTH = 1

CHUNK = 64
QUERY_BLOCK = 128
FOX_HEAD_DIM = 64
N_FOX_HEADS = (D_MODEL // 2) // FOX_HEAD_DIM
D_FOX = N_FOX_HEADS * FOX_HEAD_DIM
GDN_HEAD_DIM = 128
N_GDN_HEADS = (D_MODEL // 2) // GDN_HEAD_DIM
D_GDN = N_GDN_HEADS * GDN_HEAD_DIM
D_MIX = D_FOX + D_GDN
CONV_K = 4
D_FF = 4 * D_MODEL
EPS = 1e-6
PROJ_SIZES = (D_FOX, D_FOX, D_FOX, N_FOX_HEADS, D_GDN, D_GDN, D_GDN, N_GDN_HEADS, N_GDN_HEADS, D_GDN)
D_PROJ = sum(PROJ_SIZES)

kernel_name = "fox_gdn_hymba_sandwich_block"


def rms_norm(x, w):
    xf = x.astype(jnp.float32)
    y = xf * lax.rsqrt(jnp.mean(xf * xf, axis=-1, keepdims=True) + EPS)
    return (y * w.astype(jnp.float32)).astype(x.dtype)


def causal_depthwise_conv(x, w):
    k, c = w.shape
    return lax.conv_general_dilated(
        x, w[:, None, :].astype(x.dtype), window_strides=(1,), padding=[(k - 1, 0)],
        dimension_numbers=("NWC", "WIO", "NWC"), feature_group_count=c)


def forgetting_attention(q, k, v, log_f):
    b, s, h, d = q.shape
    scale = d ** -0.5
    cum = jnp.cumsum(log_f, axis=1).transpose(0, 2, 1)
    outs = []
    for i in range(s // QUERY_BLOCK):
        qs, qe = i * QUERY_BLOCK, (i + 1) * QUERY_BLOCK
        qb = q[:, qs:qe]
        kb = k[:, :qe]
        vb = v[:, :qe]
        logits = jnp.einsum("bqhd,bkhd->bhqk", qb, kb).astype(jnp.float32) * scale
        logits = logits + cum[:, :, qs:qe, None] - cum[:, :, None, :qe]
        mask = jnp.arange(qe)[None, :] <= (qs + jnp.arange(QUERY_BLOCK))[:, None]
        logits = jnp.where(mask, logits, -jnp.inf)
        p = jax.nn.softmax(logits, axis=-1)
        outs.append(jnp.einsum("bhqk,bkhd->bqhd", p.astype(v.dtype), vb))
    return jnp.concatenate(outs, axis=1)


def gated_delta_chunked(q, k, v, g, beta):
    b, s, h, dk = q.shape
    dv = v.shape[-1]
    n = s // CHUNK

    def to_chunks(t):
        return t.reshape(b, n, CHUNK, h, t.shape[-1]).transpose(0, 3, 1, 2, 4)

    q, k, v = to_chunks(q), to_chunks(k), to_chunks(v)
    g = g.reshape(b, n, CHUNK, h).transpose(0, 3, 1, 2)
    beta = beta.reshape(b, n, CHUNK, h).transpose(0, 3, 1, 2)
    gc = jnp.cumsum(g, axis=-1)
    incl = jnp.tril(jnp.ones((CHUNK, CHUNK), dtype=bool))
    strict = jnp.tril(jnp.ones((CHUNK, CHUNK), dtype=bool), k=-1)
    decay = jnp.exp(jnp.where(incl, gc[..., :, None] - gc[..., None, :], -jnp.inf))
    k_beta = k * beta[..., None]
    v_beta = v * beta[..., None]
    m = jnp.where(strict, jnp.einsum("bhncd,bhnkd->bhnck", k_beta, k) * decay, 0.0)
    eye = jnp.broadcast_to(jnp.eye(CHUNK, dtype=m.dtype), m.shape)
    t_inv = lax.linalg.triangular_solve(eye + m, eye, left_side=True, lower=True,
                                        unit_diagonal=True)
    u = jnp.einsum("bhnck,bhnkv->bhncv", t_inv, v_beta)
    w = jnp.einsum("bhnck,bhnkd->bhncd", t_inv, k_beta * jnp.exp(gc)[..., None])
    a_intra = jnp.einsum("bhncd,bhnkd->bhnck", q, k) * decay
    q_dec = q * jnp.exp(gc)[..., None]
    k_dec = k * jnp.exp(gc[..., -1:] - gc)[..., None]
    g_last = jnp.exp(gc[..., -1])

    def step(state, inp):
        qd, kd, uc, wc, ac, gl = inp
        v_new = uc - jnp.einsum("bhcd,bhdv->bhcv", wc, state)
        o = jnp.einsum("bhcd,bhdv->bhcv", qd, state) + jnp.einsum("bhck,bhkv->bhcv", ac, v_new)
        state = state * gl[..., None, None] + jnp.einsum("bhcd,bhcv->bhdv", kd, v_new)
        return state, o

    def chunk_major(t):
        return jnp.moveaxis(t, 2, 0)

    state0 = jnp.zeros((b, h, dk, dv), dtype=jnp.float32)
    _, o = lax.scan(step, state0, (chunk_major(q_dec), chunk_major(k_dec), chunk_major(u),
                                   chunk_major(w), chunk_major(a_intra), chunk_major(g_last)))
    return o.transpose(1, 0, 3, 2, 4).reshape(b, s, h, dv)


def _fwd_setup_inputs(seed: int = 0) -> dict:
    key = jax.random.key(seed)
    ks = jax.random.split(key, 16)
    f32 = jnp.float32

    def gain(k, n):
        return 1.0 + 0.1 * jax.random.normal(k, (n,), f32)

    x = jax.random.normal(ks[0], (BATCH, SEQ, D_MODEL), f32)
    pre_mix_norm = gain(ks[1], D_MODEL)
    w_in = jax.random.normal(ks[2], (D_MODEL, D_PROJ), f32) * D_MODEL ** -0.5
    fox_f_bias = 3.0 + 0.5 * jax.random.normal(ks[3], (N_FOX_HEADS,), f32)
    fox_out_norm = gain(ks[4], FOX_HEAD_DIM)
    gdn_conv_w = jax.random.normal(ks[5], (CONV_K, 3 * D_GDN), f32) * CONV_K ** -0.5
    gdn_a_log = jnp.log(jax.random.uniform(ks[6], (N_GDN_HEADS,), f32, 1.0, 16.0))
    dt = jnp.exp(jax.random.uniform(ks[7], (N_GDN_HEADS,), f32, np.log(1e-3), np.log(1e-1)))
    gdn_dt_bias = dt + jnp.log(-jnp.expm1(-dt))
    gdn_out_norm = gain(ks[8], GDN_HEAD_DIM)
    w_out = jax.random.normal(ks[9], (D_MIX, D_MODEL), f32) * D_MIX ** -0.5
    post_mix_norm = gain(ks[10], D_MODEL)
    pre_mlp_norm = gain(ks[11], D_MODEL)
    w_up = jax.random.normal(ks[12], (D_MODEL, D_FF), f32) * D_MODEL ** -0.5
    w_down = jax.random.normal(ks[13], (D_FF, D_MODEL), f32) * D_FF ** -0.5
    post_mlp_norm = gain(ks[14], D_MODEL)
    return {"x": x, "pre_mix_norm": pre_mix_norm, "w_in": w_in, "fox_f_bias": fox_f_bias,
            "fox_out_norm": fox_out_norm, "gdn_conv_w": gdn_conv_w, "gdn_a_log": gdn_a_log,
            "gdn_dt_bias": gdn_dt_bias, "gdn_out_norm": gdn_out_norm, "w_out": w_out,
            "post_mix_norm": post_mix_norm, "pre_mlp_norm": pre_mlp_norm, "w_up": w_up,
            "w_down": w_down, "post_mlp_norm": post_mlp_norm}


def _fwd_reference(x, pre_mix_norm, w_in, fox_f_bias, fox_out_norm, gdn_conv_w, gdn_a_log,
              gdn_dt_bias, gdn_out_norm, w_out, post_mix_norm, pre_mlp_norm, w_up, w_down,
              post_mlp_norm):
    b, s, _ = x.shape
    split_at = [int(v) for v in np.cumsum(PROJ_SIZES)[:-1]]
    for _layer in range(DEPTH):
        h = rms_norm(x, pre_mix_norm)
        proj = h @ w_in
        fq, fk, fv, ff, gq, gk, gv, gb, ga, gz = jnp.split(proj, split_at, axis=-1)

        fq = fq.reshape(b, s, N_FOX_HEADS, FOX_HEAD_DIM)
        fk = fk.reshape(b, s, N_FOX_HEADS, FOX_HEAD_DIM)
        fv = fv.reshape(b, s, N_FOX_HEADS, FOX_HEAD_DIM)
        log_f = jax.nn.log_sigmoid(ff.astype(jnp.float32) + fox_f_bias.astype(jnp.float32))
        fox_o = forgetting_attention(fq, fk, fv, log_f)
        fox_o = rms_norm(fox_o, fox_out_norm).reshape(b, s, D_FOX)

        qkv = jax.nn.silu(causal_depthwise_conv(jnp.concatenate([gq, gk, gv], axis=-1), gdn_conv_w))
        qkv = qkv.astype(jnp.float32)
        cq, ck, cv = jnp.split(qkv, [D_GDN, 2 * D_GDN], axis=-1)
        cq = cq.reshape(b, s, N_GDN_HEADS, GDN_HEAD_DIM)
        ck = ck.reshape(b, s, N_GDN_HEADS, GDN_HEAD_DIM)
        cv = cv.reshape(b, s, N_GDN_HEADS, GDN_HEAD_DIM)
        cq = cq * lax.rsqrt(jnp.sum(cq * cq, axis=-1, keepdims=True) + EPS) * GDN_HEAD_DIM ** -0.5
        ck = ck * lax.rsqrt(jnp.sum(ck * ck, axis=-1, keepdims=True) + EPS)
        beta = jax.nn.sigmoid(gb.astype(jnp.float32))
        g = -jnp.exp(gdn_a_log.astype(jnp.float32)) * jax.nn.softplus(
            ga.astype(jnp.float32) + gdn_dt_bias.astype(jnp.float32))
        gdn_o = gated_delta_chunked(cq, ck, cv, g, beta)
        gate = jax.nn.silu(gz.astype(jnp.float32)).reshape(b, s, N_GDN_HEADS, GDN_HEAD_DIM)
        gdn_o = (rms_norm(gdn_o, gdn_out_norm) * gate).astype(x.dtype).reshape(b, s, D_GDN)

        mixed = jnp.concatenate([fox_o, gdn_o], axis=-1) @ w_out
        x = x + rms_norm(mixed, post_mix_norm)

        h = rms_norm(x, pre_mlp_norm)
        y = jnp.square(jax.nn.relu(h @ w_up)) @ w_down
        x = x + rms_norm(y, post_mlp_norm)
    return x


import jax as _jax
import jax.numpy as _jnp

TWIN_FORMAT = 'train_step'
FWD_PARAMS = ['x', 'pre_mix_norm', 'w_in', 'fox_f_bias', 'fox_out_norm', 'gdn_conv_w', 'gdn_a_log', 'gdn_dt_bias', 'gdn_out_norm', 'w_out', 'post_mix_norm', 'pre_mlp_norm', 'w_up', 'w_down', 'post_mlp_norm']
TWIN_WEIGHTS = ['pre_mix_norm', 'w_in', 'fox_f_bias', 'fox_out_norm', 'gdn_conv_w', 'gdn_a_log', 'gdn_dt_bias', 'gdn_out_norm', 'w_out', 'post_mix_norm', 'pre_mlp_norm', 'w_up', 'w_down', 'post_mlp_norm']
TWIN_DIFF_INPUT = 'x'
TWIN_INPUTS = ['x', 'pre_mix_norm', 'w_in', 'fox_f_bias', 'fox_out_norm', 'gdn_conv_w', 'gdn_a_log', 'gdn_dt_bias', 'gdn_out_norm', 'w_out', 'post_mix_norm', 'pre_mlp_norm', 'w_up', 'w_down', 'post_mlp_norm', 'loss_target', 'm_pre_mix_norm', 'm_w_in', 'm_fox_f_bias', 'm_fox_out_norm', 'm_gdn_conv_w', 'm_gdn_a_log', 'm_gdn_dt_bias', 'm_gdn_out_norm', 'm_w_out', 'm_post_mix_norm', 'm_pre_mlp_norm', 'm_w_up', 'm_w_down', 'm_post_mlp_norm', 'v_pre_mix_norm', 'v_w_in', 'v_fox_f_bias', 'v_fox_out_norm', 'v_gdn_conv_w', 'v_gdn_a_log', 'v_gdn_dt_bias', 'v_gdn_out_norm', 'v_w_out', 'v_post_mix_norm', 'v_pre_mlp_norm', 'v_w_up', 'v_w_down', 'v_post_mlp_norm']
TWIN_OUTPUTS = ['loss', 'grad_x', 'grad_pre_mix_norm', 'grad_w_in', 'grad_fox_f_bias', 'grad_fox_out_norm', 'grad_gdn_conv_w', 'grad_gdn_a_log', 'grad_gdn_dt_bias', 'grad_gdn_out_norm', 'grad_w_out', 'grad_post_mix_norm', 'grad_pre_mlp_norm', 'grad_w_up', 'grad_w_down', 'grad_post_mlp_norm', 'delta_pre_mix_norm', 'delta_w_in', 'delta_fox_f_bias', 'delta_fox_out_norm', 'delta_gdn_conv_w', 'delta_gdn_a_log', 'delta_gdn_dt_bias', 'delta_gdn_out_norm', 'delta_w_out', 'delta_post_mix_norm', 'delta_pre_mlp_norm', 'delta_w_up', 'delta_w_down', 'delta_post_mlp_norm', 'new_m_pre_mix_norm', 'new_m_w_in', 'new_m_fox_f_bias', 'new_m_fox_out_norm', 'new_m_gdn_conv_w', 'new_m_gdn_a_log', 'new_m_gdn_dt_bias', 'new_m_gdn_out_norm', 'new_m_w_out', 'new_m_post_mix_norm', 'new_m_pre_mlp_norm', 'new_m_w_up', 'new_m_w_down', 'new_m_post_mlp_norm', 'new_v_pre_mix_norm', 'new_v_w_in', 'new_v_fox_f_bias', 'new_v_fox_out_norm', 'new_v_gdn_conv_w', 'new_v_gdn_a_log', 'new_v_gdn_dt_bias', 'new_v_gdn_out_norm', 'new_v_w_out', 'new_v_post_mix_norm', 'new_v_pre_mlp_norm', 'new_v_w_up', 'new_v_w_down', 'new_v_post_mlp_norm']
TWIN_LEAF_KINDS = {'loss': 'loss', 'grad_x': 'grad_x', 'grad_pre_mix_norm': 'grad_w', 'grad_w_in': 'grad_w', 'grad_fox_f_bias': 'grad_w', 'grad_fox_out_norm': 'grad_w', 'grad_gdn_conv_w': 'grad_w', 'grad_gdn_a_log': 'grad_w', 'grad_gdn_dt_bias': 'grad_w', 'grad_gdn_out_norm': 'grad_w', 'grad_w_out': 'grad_w', 'grad_post_mix_norm': 'grad_w', 'grad_pre_mlp_norm': 'grad_w', 'grad_w_up': 'grad_w', 'grad_w_down': 'grad_w', 'grad_post_mlp_norm': 'grad_w', 'delta_pre_mix_norm': 'delta_w', 'delta_w_in': 'delta_w', 'delta_fox_f_bias': 'delta_w', 'delta_fox_out_norm': 'delta_w', 'delta_gdn_conv_w': 'delta_w', 'delta_gdn_a_log': 'delta_w', 'delta_gdn_dt_bias': 'delta_w', 'delta_gdn_out_norm': 'delta_w', 'delta_w_out': 'delta_w', 'delta_post_mix_norm': 'delta_w', 'delta_pre_mlp_norm': 'delta_w', 'delta_w_up': 'delta_w', 'delta_w_down': 'delta_w', 'delta_post_mlp_norm': 'delta_w', 'new_m_pre_mix_norm': 'new_m', 'new_m_w_in': 'new_m', 'new_m_fox_f_bias': 'new_m', 'new_m_fox_out_norm': 'new_m', 'new_m_gdn_conv_w': 'new_m', 'new_m_gdn_a_log': 'new_m', 'new_m_gdn_dt_bias': 'new_m', 'new_m_gdn_out_norm': 'new_m', 'new_m_w_out': 'new_m', 'new_m_post_mix_norm': 'new_m', 'new_m_pre_mlp_norm': 'new_m', 'new_m_w_up': 'new_m', 'new_m_w_down': 'new_m', 'new_m_post_mlp_norm': 'new_m', 'new_v_pre_mix_norm': 'new_v', 'new_v_w_in': 'new_v', 'new_v_fox_f_bias': 'new_v', 'new_v_fox_out_norm': 'new_v', 'new_v_gdn_conv_w': 'new_v', 'new_v_gdn_a_log': 'new_v', 'new_v_gdn_dt_bias': 'new_v', 'new_v_gdn_out_norm': 'new_v', 'new_v_w_out': 'new_v', 'new_v_post_mix_norm': 'new_v', 'new_v_pre_mlp_norm': 'new_v', 'new_v_w_up': 'new_v', 'new_v_w_down': 'new_v', 'new_v_post_mlp_norm': 'new_v'}


def _forward(args):
    return _fwd_reference(*[args[k] for k in FWD_PARAMS])


def _output_shape():
    out = _jax.eval_shape(lambda: _forward(_fwd_setup_inputs(0)))
    return out.shape, out.dtype

N_MICROBATCH = 1
ADAM_LR = 0.001
ADAM_B1 = 0.9
ADAM_B2 = 0.999
ADAM_EPS = 1e-08
ADAM_WD = 0.01
ADAM_STEP = 10
PER_EXAMPLE_BATCH_AXIS = {'x': 0, 'loss_target': 0}
SHARED_INPUTS = []
_WEIGHT_DTYPES = {'pre_mix_norm': _jnp.float32, 'w_in': _jnp.float32, 'fox_f_bias': _jnp.float32, 'fox_out_norm': _jnp.float32, 'gdn_conv_w': _jnp.float32, 'gdn_a_log': _jnp.float32, 'gdn_dt_bias': _jnp.float32, 'gdn_out_norm': _jnp.float32, 'w_out': _jnp.float32, 'post_mix_norm': _jnp.float32, 'pre_mlp_norm': _jnp.float32, 'w_up': _jnp.float32, 'w_down': _jnp.float32, 'post_mlp_norm': _jnp.float32}
MOMENT_SCALE = {'pre_mix_norm': 5.652977e-01, 'w_in': 2.935551e-01, 'fox_f_bias': 3.440527e+00, 'fox_out_norm': 2.430064e+00, 'gdn_conv_w': 4.015850e-01, 'gdn_a_log': 1.435641e+00, 'gdn_dt_bias': 1.410651e+00, 'gdn_out_norm': 2.006316e+00, 'w_out': 7.952064e-01, 'post_mix_norm': 1.613662e+01, 'pre_mlp_norm': 4.912011e-01, 'w_up': 2.626319e-01, 'w_down': 9.879426e-01, 'post_mlp_norm': 1.653209e+01}


def _to_microbatches(a, axis):
    t = _jnp.moveaxis(a, axis, 0)
    t = t.reshape((N_MICROBATCH, t.shape[0] // N_MICROBATCH) + t.shape[1:])
    return _jnp.moveaxis(t, 1, axis + 1)


def setup_inputs(seed: int = 0) -> dict:
    inp = _fwd_setup_inputs(seed)
    key = _jax.random.fold_in(_jax.random.key(seed), 7919)
    shape, _ = _output_shape()
    out = dict(inp)
    out["loss_target"] = _jax.random.normal(_jax.random.fold_in(key, 0), shape, _jnp.float32)
    for i, name in enumerate(TWIN_WEIGHTS):
        w = inp[name].astype(_jnp.float32)
        if MOMENT_SCALE is None:
            s = _jnp.sqrt(_jnp.mean(_jnp.square(w)) + 1e-30)
        else:
            s = MOMENT_SCALE[name]
        km, kv = _jax.random.split(_jax.random.fold_in(key, i + 1))
        out[name] = w
        out["m_" + name] = s * _jax.random.normal(km, w.shape, _jnp.float32)
        out["v_" + name] = (s * s) * _jax.random.uniform(kv, w.shape, _jnp.float32, 0.5, 1.5)
    if N_MICROBATCH > 1:
        for name, axis in PER_EXAMPLE_BATCH_AXIS.items():
            out[name] = _to_microbatches(out[name], axis)
    return {'x': out['x'], 'pre_mix_norm': out['pre_mix_norm'], 'w_in': out['w_in'], 'fox_f_bias': out['fox_f_bias'], 'fox_out_norm': out['fox_out_norm'], 'gdn_conv_w': out['gdn_conv_w'], 'gdn_a_log': out['gdn_a_log'], 'gdn_dt_bias': out['gdn_dt_bias'], 'gdn_out_norm': out['gdn_out_norm'], 'w_out': out['w_out'], 'post_mix_norm': out['post_mix_norm'], 'pre_mlp_norm': out['pre_mlp_norm'], 'w_up': out['w_up'], 'w_down': out['w_down'], 'post_mlp_norm': out['post_mlp_norm'], 'loss_target': out['loss_target'], 'm_pre_mix_norm': out['m_pre_mix_norm'], 'm_w_in': out['m_w_in'], 'm_fox_f_bias': out['m_fox_f_bias'], 'm_fox_out_norm': out['m_fox_out_norm'], 'm_gdn_conv_w': out['m_gdn_conv_w'], 'm_gdn_a_log': out['m_gdn_a_log'], 'm_gdn_dt_bias': out['m_gdn_dt_bias'], 'm_gdn_out_norm': out['m_gdn_out_norm'], 'm_w_out': out['m_w_out'], 'm_post_mix_norm': out['m_post_mix_norm'], 'm_pre_mlp_norm': out['m_pre_mlp_norm'], 'm_w_up': out['m_w_up'], 'm_w_down': out['m_w_down'], 'm_post_mlp_norm': out['m_post_mlp_norm'], 'v_pre_mix_norm': out['v_pre_mix_norm'], 'v_w_in': out['v_w_in'], 'v_fox_f_bias': out['v_fox_f_bias'], 'v_fox_out_norm': out['v_fox_out_norm'], 'v_gdn_conv_w': out['v_gdn_conv_w'], 'v_gdn_a_log': out['v_gdn_a_log'], 'v_gdn_dt_bias': out['v_gdn_dt_bias'], 'v_gdn_out_norm': out['v_gdn_out_norm'], 'v_w_out': out['v_w_out'], 'v_post_mix_norm': out['v_post_mix_norm'], 'v_pre_mlp_norm': out['v_pre_mlp_norm'], 'v_w_up': out['v_w_up'], 'v_w_down': out['v_w_down'], 'v_post_mlp_norm': out['v_post_mlp_norm']}


def _loss(weights, diff, rest, loss_target):
    with _jax.named_scope("forward"):
        args = {**rest, TWIN_DIFF_INPUT: diff, **{k: w.astype(_WEIGHT_DTYPES[k]) for k, w in weights.items()}}
        y = _forward(args)
    with _jax.named_scope("loss_head"):
        err = _jnp.square(y.astype(_jnp.float32) - loss_target)
        return 0.5 * _jnp.sum(_jnp.mean(err, axis=-1)) if err.ndim else 0.5 * err


def _adamw(w, g, m, v):
    m = ADAM_B1 * m + (1.0 - ADAM_B1) * g
    v = ADAM_B2 * v + (1.0 - ADAM_B2) * _jnp.square(g)
    m_hat = m / (1.0 - ADAM_B1 ** ADAM_STEP)
    v_hat = v / (1.0 - ADAM_B2 ** ADAM_STEP)
    delta = -ADAM_LR * (m_hat / (_jnp.sqrt(v_hat) + ADAM_EPS) + ADAM_WD * w)
    return delta, m, v


def reference(x, pre_mix_norm, w_in, fox_f_bias, fox_out_norm, gdn_conv_w, gdn_a_log, gdn_dt_bias, gdn_out_norm, w_out, post_mix_norm, pre_mlp_norm, w_up, w_down, post_mlp_norm, loss_target, m_pre_mix_norm, m_w_in, m_fox_f_bias, m_fox_out_norm, m_gdn_conv_w, m_gdn_a_log, m_gdn_dt_bias, m_gdn_out_norm, m_w_out, m_post_mix_norm, m_pre_mlp_norm, m_w_up, m_w_down, m_post_mlp_norm, v_pre_mix_norm, v_w_in, v_fox_f_bias, v_fox_out_norm, v_gdn_conv_w, v_gdn_a_log, v_gdn_dt_bias, v_gdn_out_norm, v_w_out, v_post_mix_norm, v_pre_mlp_norm, v_w_up, v_w_down, v_post_mlp_norm):
    given = dict(x=x, pre_mix_norm=pre_mix_norm, w_in=w_in, fox_f_bias=fox_f_bias, fox_out_norm=fox_out_norm, gdn_conv_w=gdn_conv_w, gdn_a_log=gdn_a_log, gdn_dt_bias=gdn_dt_bias, gdn_out_norm=gdn_out_norm, w_out=w_out, post_mix_norm=post_mix_norm, pre_mlp_norm=pre_mlp_norm, w_up=w_up, w_down=w_down, post_mlp_norm=post_mlp_norm, loss_target=loss_target, m_pre_mix_norm=m_pre_mix_norm, m_w_in=m_w_in, m_fox_f_bias=m_fox_f_bias, m_fox_out_norm=m_fox_out_norm, m_gdn_conv_w=m_gdn_conv_w, m_gdn_a_log=m_gdn_a_log, m_gdn_dt_bias=m_gdn_dt_bias, m_gdn_out_norm=m_gdn_out_norm, m_w_out=m_w_out, m_post_mix_norm=m_post_mix_norm, m_pre_mlp_norm=m_pre_mlp_norm, m_w_up=m_w_up, m_w_down=m_w_down, m_post_mlp_norm=m_post_mlp_norm, v_pre_mix_norm=v_pre_mix_norm, v_w_in=v_w_in, v_fox_f_bias=v_fox_f_bias, v_fox_out_norm=v_fox_out_norm, v_gdn_conv_w=v_gdn_conv_w, v_gdn_a_log=v_gdn_a_log, v_gdn_dt_bias=v_gdn_dt_bias, v_gdn_out_norm=v_gdn_out_norm, v_w_out=v_w_out, v_post_mix_norm=v_post_mix_norm, v_pre_mlp_norm=v_pre_mlp_norm, v_w_up=v_w_up, v_w_down=v_w_down, v_post_mlp_norm=v_post_mlp_norm)
    weights = {n: given[n] for n in TWIN_WEIGHTS}
    shared = {n: given[n] for n in SHARED_INPUTS}
    per_example = {n: given[n] for n in ['x']}
    grad_fn = _jax.value_and_grad(_loss, argnums=(0, 1))

    def one_microbatch(ex, loss_target):
        ex = dict(ex)
        diff = ex.pop(TWIN_DIFF_INPUT)
        return grad_fn(weights, diff, {**shared, **ex}, loss_target)

    if N_MICROBATCH == 1:
        loss, (grad_w, grad_x) = one_microbatch(per_example, given["loss_target"])
    else:
        def body(carry, xs):
            loss_sum, grad_sum = carry
            l_k, (gw_k, gx_k) = one_microbatch(xs[0], xs[1])
            with _jax.named_scope("update"):
                return (loss_sum + l_k, _jax.tree.map(_jnp.add, grad_sum, gw_k)), gx_k

        init = (_jnp.zeros((), _jnp.float32), _jax.tree.map(_jnp.zeros_like, weights))
        (loss, grad_w), grad_x = _jax.lax.scan(body, init, (per_example, given["loss_target"]))
    with _jax.named_scope("update"):
        delta_w, new_m, new_v = {}, {}, {}
        for n in TWIN_WEIGHTS:
            delta_w[n], new_m[n], new_v[n] = _adamw(weights[n], grad_w[n], given["m_" + n], given["v_" + n])
    return (loss, grad_x, *[grad_w[n] for n in TWIN_WEIGHTS], *[delta_w[n] for n in TWIN_WEIGHTS],
            *[new_m[n] for n in TWIN_WEIGHTS], *[new_v[n] for n in TWIN_WEIGHTS])
```

```python
import functools

import jax
import jax.numpy as jnp
from jax import lax
from jax.experimental import pallas as pl
from jax.experimental.pallas import tpu as pltpu

F32 = jnp.float32
BF16 = jnp.bfloat16
MESH = pl.DeviceIdType.MESH

S = 2048
D = 1024
NFH, FHD = 8, 64
NGH, GHD = 4, 128
DFOX = NFH * FHD
DGDN = NGH * GHD
CHUNK = 64
NCH = S // CHUNK
CONV_K = 4
DFF = 4 * D
EPS = 1e-6
DPROJ = 3600
DPROJ_PAD = 3840
COL_GZ = 3072
COL_SMALL = 3584
NCHIP = 4
NDEV = 8
LANES = 128
ROWS_WIN, ROWS_WOUT, ROWS_WUP, ROWS_WDOWN = 3600, 1024, 4096, 4096
ROWS_HALF = ROWS_WIN + ROWS_WOUT + ROWS_WUP + ROWS_WDOWN
VMEM_LIMIT = 48 * 1024 * 1024
TRP = 2136

ADAM_LR = 0.001
ADAM_B1 = 0.9
ADAM_B2 = 0.999
ADAM_EPS = 1e-08
ADAM_WD = 0.01
ADAM_STEP = 10


def _cparams(**kw):
    return pltpu.CompilerParams(vmem_limit_bytes=VMEM_LIMIT, **kw)


def _dn(ca, cb):
    return (((ca,), (cb,)), ((), ()))


def _dot(a, b, ca=1, cb=0):
    return lax.dot_general(a.astype(BF16), b.astype(BF16), _dn(ca, cb), preferred_element_type=F32)


def _hdot(a, b, ca=1, cb=0):
    return lax.dot_general(a.astype(F32), b.astype(F32), _dn(ca, cb), precision=lax.Precision.HIGHEST,
                           preferred_element_type=F32)


@jax.custom_vjp
def _mm_nn(a, b):
    return _dot(a, b, 1, 0)


def _mm_nn_fwd(a, b):
    return _dot(a, b, 1, 0), (a, b)


def _mm_nn_bwd(res, g):
    a, b = res
    return _dot(g, b, 1, 1), _dot(a, g, 0, 0)


_mm_nn.defvjp(_mm_nn_fwd, _mm_nn_bwd)


@jax.custom_vjp
def _mm_nt(a, b):
    return _dot(a, b, 1, 1)


def _mm_nt_fwd(a, b):
    return _dot(a, b, 1, 1), (a, b)


def _mm_nt_bwd(res, g):
    a, b = res
    return _dot(g, b, 1, 0), _dot(g, a, 0, 0)


_mm_nt.defvjp(_mm_nt_fwd, _mm_nt_bwd)


@jax.custom_vjp
def _hm_nn(a, b):
    return _hdot(a, b, 1, 0)


def _hm_nn_fwd(a, b):
    return _hdot(a, b, 1, 0), (a, b)


def _hm_nn_bwd(res, g):
    a, b = res
    return _hdot(g, b, 1, 1), _hdot(a, g, 0, 0)


_hm_nn.defvjp(_hm_nn_fwd, _hm_nn_bwd)


def _sigmoid(z):
    return 1.0 / (1.0 + jnp.exp(-z))


def _softplus(z):
    return jnp.maximum(z, 0.0) + jnp.log(1.0 + jnp.exp(-jnp.abs(z)))


def _silu(z):
    return z * _sigmoid(z)


def _rms_scale(x):
    return lax.rsqrt(jnp.mean(x * x, axis=-1, keepdims=True) + EPS)


def _rms_bwd(x, w, g):
    r = _rms_scale(x)
    gw = g * w
    dx = r * gw - x * (r * r * r) * jnp.mean(gw * x, axis=-1, keepdims=True)
    return dx, g * x * r


def _matmul(a, b, *, ta=False, tb=False, out_dtype=F32, tm=512, tn=512, tk=512, name):
    m, k = (a.shape[1], a.shape[0]) if ta else a.shape
    n = b.shape[0] if tb else b.shape[1]
    assert (b.shape[1] if tb else b.shape[0]) == k
    tm, tn, tk = min(tm, m), min(tn, n), min(tk, k)
    assert m % tm == 0 and n % tn == 0 and k % tk == 0, (m, n, k, tm, tn, tk)
    nk = k // tk

    def body(a_ref, b_ref, o_ref, acc_ref):
        kk = pl.program_id(2)

        @pl.when(kk == 0)
        def _():
            acc_ref[...] = jnp.zeros_like(acc_ref)

        acc_ref[...] += _dot(a_ref[...], b_ref[...], 0 if ta else 1, 1 if tb else 0)

        @pl.when(kk == nk - 1)
        def _():
            o_ref[...] = acc_ref[...].astype(o_ref.dtype)

    a_spec = pl.BlockSpec((tk, tm), lambda i, j, kk: (kk, i)) if ta else pl.BlockSpec((tm, tk), lambda i, j, kk: (i, kk))
    b_spec = pl.BlockSpec((tn, tk), lambda i, j, kk: (j, kk)) if tb else pl.BlockSpec((tk, tn), lambda i, j, kk: (kk, j))
    return pl.pallas_call(
        body, name=name, grid=(m // tm, n // tn, nk),
        in_specs=[a_spec, b_spec], out_specs=pl.BlockSpec((tm, tn), lambda i, j, kk: (i, j)),
        out_shape=jax.ShapeDtypeStruct((m, n), out_dtype),
        scratch_shapes=[pltpu.VMEM((tm, tn), F32)],
        compiler_params=_cparams(dimension_semantics=("parallel", "parallel", "arbitrary")),
    )(a, b)


TR = 256


def _row_spec(cols):
    return pl.BlockSpec((TR, cols), lambda i: (i, 0))


def _vec_spec(cols):
    return pl.BlockSpec((1, cols), lambda i: (0, 0))


def _pre_norm(x, w):
    def body(x_ref, w_ref, h_ref):
        xv = x_ref[...]
        h_ref[...] = (xv * _rms_scale(xv) * w_ref[...]).astype(BF16)

    return pl.pallas_call(
        body, name="pre_norm", grid=(S // TR,), in_specs=[_row_spec(D), _vec_spec(D)], out_specs=_row_spec(D),
        out_shape=jax.ShapeDtypeStruct((S, D), BF16), compiler_params=_cparams(),
    )(x, w.reshape(1, D))


def _post_mix(x, mixed, w_post, w_pre_mlp):
    def body(x_ref, m_ref, wp_ref, wm_ref, x1_ref, h2_ref):
        mv = m_ref[...]
        x1 = x_ref[...] + mv * _rms_scale(mv) * wp_ref[...]
        x1_ref[...] = x1
        h2_ref[...] = (x1 * _rms_scale(x1) * wm_ref[...]).astype(BF16)

    return pl.pallas_call(
        body, name="post_mix", grid=(S // TR,),
        in_specs=[_row_spec(D), _row_spec(D), _vec_spec(D), _vec_spec(D)], out_specs=[_row_spec(D), _row_spec(D)],
        out_shape=[jax.ShapeDtypeStruct((S, D), F32), jax.ShapeDtypeStruct((S, D), BF16)], compiler_params=_cparams(),
    )(x, mixed, w_post.reshape(1, D), w_pre_mlp.reshape(1, D))


def _relu2(u):
    def body(u_ref, a_ref):
        r = jnp.maximum(u_ref[...], 0.0)
        a_ref[...] = (r * r).astype(BF16)

    return pl.pallas_call(
        body, name="relu2", grid=(S // TR,), in_specs=[_row_spec(DFF)], out_specs=_row_spec(DFF),
        out_shape=jax.ShapeDtypeStruct((S, DFF), BF16), compiler_params=_cparams(),
    )(u)


def _relu2_bwd(da, u):
    def body(da_ref, u_ref, du_ref):
        du_ref[...] = (da_ref[...] * 2.0 * jnp.maximum(u_ref[...], 0.0)).astype(BF16)

    return pl.pallas_call(
        body, name="relu2_bwd", grid=(S // TR,), in_specs=[_row_spec(DFF), _row_spec(DFF)], out_specs=_row_spec(DFF),
        out_shape=jax.ShapeDtypeStruct((S, DFF), BF16), compiler_params=_cparams(),
    )(da, u)


def _loss_head(x1, y, w_post_mlp, target):
    def body(x1_ref, y_ref, w_ref, t_ref, dx2_ref, dy_ref, dw_ref, loss_ref):
        i = pl.program_id(0)
        yv = y_ref[...]
        w = w_ref[...]
        x2 = x1_ref[...] + yv * _rms_scale(yv) * w
        err = x2 - t_ref[...]
        dx2 = err * (1.0 / D)
        dx2_ref[...] = dx2
        dy, dwt = _rms_bwd(yv, w, dx2)
        dy_ref[...] = dy.astype(BF16)

        @pl.when(i == 0)
        def _():
            dw_ref[...] = jnp.zeros_like(dw_ref)
            loss_ref[...] = jnp.zeros_like(loss_ref)

        dw_ref[...] += jnp.sum(dwt, axis=0, keepdims=True)
        part = 0.5 * jnp.sum(jnp.mean(err * err, axis=-1, keepdims=True), axis=0, keepdims=True)
        loss_ref[...] += jnp.broadcast_to(part, loss_ref.shape)

    return pl.pallas_call(
        body, name="loss_head", grid=(S // TR,),
        in_specs=[_row_spec(D), _row_spec(D), _vec_spec(D), _row_spec(D)],
        out_specs=[_row_spec(D), _row_spec(D), _vec_spec(D), _vec_spec(LANES)],
        out_shape=[jax.ShapeDtypeStruct((S, D), F32), jax.ShapeDtypeStruct((S, D), BF16),
                   jax.ShapeDtypeStruct((1, D), F32), jax.ShapeDtypeStruct((1, LANES), F32)],
        compiler_params=_cparams(),
    )(x1, y, w_post_mlp.reshape(1, D), target)


def _mid_bwd(dh2, x1, w_pre_mlp, dx2, mixed, w_post):
    def body(dh2_ref, x1_ref, wm_ref, dx2_ref, m_ref, wp_ref, dx1_ref, dm_ref, dwm_ref, dwp_ref):
        i = pl.program_id(0)
        dxa, dwm = _rms_bwd(x1_ref[...], wm_ref[...], dh2_ref[...])
        dx1 = dx2_ref[...] + dxa
        dx1_ref[...] = dx1
        dm, dwp = _rms_bwd(m_ref[...], wp_ref[...], dx1)
        dm_ref[...] = dm.astype(BF16)

        @pl.when(i == 0)
        def _():
            dwm_ref[...] = jnp.zeros_like(dwm_ref)
            dwp_ref[...] = jnp.zeros_like(dwp_ref)

        dwm_ref[...] += jnp.sum(dwm, axis=0, keepdims=True)
        dwp_ref[...] += jnp.sum(dwp, axis=0, keepdims=True)

    return pl.pallas_call(
        body, name="mid_bwd", grid=(S // TR,),
        in_specs=[_row_spec(D), _row_spec(D), _vec_spec(D), _row_spec(D), _row_spec(D), _vec_spec(D)],
        out_specs=[_row_spec(D), _row_spec(D), _vec_spec(D), _vec_spec(D)],
        out_shape=[jax.ShapeDtypeStruct((S, D), F32), jax.ShapeDtypeStruct((S, D), BF16),
                   jax.ShapeDtypeStruct((1, D), F32), jax.ShapeDtypeStruct((1, D), F32)],
        compiler_params=_cparams(),
    )(dh2, x1, w_pre_mlp.reshape(1, D), dx2, mixed, w_post.reshape(1, D))


def _pre_norm_bwd(dh, x, w, dx1):
    def body(dh_ref, x_ref, w_ref, dx1_ref, dx_ref, dw_ref):
        i = pl.program_id(0)
        dxa, dwt = _rms_bwd(x_ref[...], w_ref[...], dh_ref[...])
        dx_ref[...] = dx1_ref[...] + dxa

        @pl.when(i == 0)
        def _():
            dw_ref[...] = jnp.zeros_like(dw_ref)

        dw_ref[...] += jnp.sum(dwt, axis=0, keepdims=True)

    return pl.pallas_call(
        body, name="pre_norm_bwd", grid=(S // TR,),
        in_specs=[_row_spec(D), _row_spec(D), _vec_spec(D), _row_spec(D)], out_specs=[_row_spec(D), _vec_spec(D)],
        out_shape=[jax.ShapeDtypeStruct((S, D), F32), jax.ShapeDtypeStruct((1, D), F32)], compiler_params=_cparams(),
    )(dh, x, w.reshape(1, D), dx1)


TG = 128
LANE_BETA, LANE_G = 8, 12


def _gate_lanes(shape):
    lane = lax.broadcasted_iota(jnp.int32, shape, 1)
    return lane < LANE_BETA, (lane >= LANE_BETA) & (lane < LANE_G), (lane >= LANE_G) & (lane < LANE_G + NGH)


def _gates(small, bias_vec, alog_vec):
    def body(s_ref, b_ref, a_ref, o_ref, carry_ref):
        i = pl.program_id(0)

        @pl.when(i == 0)
        def _():
            carry_ref[...] = jnp.zeros_like(carry_ref)

        z = s_ref[...] + b_ref[...]
        tail = jnp.log(1.0 + jnp.exp(-jnp.abs(z)))
        sp = jnp.maximum(z, 0.0) + tail
        lf = jnp.minimum(z, 0.0) - tail
        r = lax.broadcasted_iota(jnp.int32, (TG, TG), 0)
        c = lax.broadcasted_iota(jnp.int32, (TG, TG), 1)
        tri = (c <= r).astype(F32)
        cum = _hdot(tri, lf) + carry_ref[...]
        carry_ref[...] = cum[TG - 1:TG, :]
        is_fox, is_beta, is_g = _gate_lanes(z.shape)
        o_ref[...] = jnp.where(is_fox, cum, jnp.where(is_beta, _sigmoid(z), jnp.where(is_g, -jnp.exp(a_ref[...]) * sp, 0.0)))

    spec = pl.BlockSpec((TG, LANES), lambda i: (i, 0))
    return pl.pallas_call(
        body, name="gates", grid=(S // TG,), in_specs=[spec, _vec_spec(LANES), _vec_spec(LANES)], out_specs=spec,
        out_shape=jax.ShapeDtypeStruct((S, LANES), F32), scratch_shapes=[pltpu.VMEM((1, LANES), F32)],
        compiler_params=_cparams(),
    )(small, bias_vec, alog_vec)


def _gates_bwd(small, bias_vec, alog_vec, dgates):
    nb = S // TG

    def body(s_ref, b_ref, a_ref, dg_ref, ds_ref, red_ref, carry_ref):
        i = pl.program_id(0)

        @pl.when(i == 0)
        def _():
            carry_ref[...] = jnp.zeros_like(carry_ref)
            red_ref[...] = jnp.zeros_like(red_ref)

        z = s_ref[...] + b_ref[...]
        dg = dg_ref[...]
        r = lax.broadcasted_iota(jnp.int32, (TG, TG), 0)
        c = lax.broadcasted_iota(jnp.int32, (TG, TG), 1)
        upper = (c >= r).astype(F32)
        dlf = _hdot(upper, dg) + carry_ref[...]
        carry_ref[...] = dlf[0:1, :]
        sig = _sigmoid(z)
        g_scale = -jnp.exp(a_ref[...])
        is_fox, is_beta, is_g = _gate_lanes(z.shape)
        ds = jnp.where(is_fox, dlf * (1.0 - sig), jnp.where(is_beta, dg * sig * (1.0 - sig), jnp.where(is_g, dg * g_scale * sig, 0.0)))
        ds_ref[...] = ds
        dalog = jnp.where(is_g, dg * g_scale * _softplus(z), 0.0)
        red_ref[0:1, :] += jnp.sum(ds, axis=0, keepdims=True)
        red_ref[1:2, :] += jnp.sum(dalog, axis=0, keepdims=True)

    spec = pl.BlockSpec((TG, LANES), lambda i: (nb - 1 - i, 0))
    return pl.pallas_call(
        body, name="gates_bwd", grid=(nb,), in_specs=[spec, _vec_spec(LANES), _vec_spec(LANES), spec],
        out_specs=[spec, pl.BlockSpec((8, LANES), lambda i: (0, 0))],
        out_shape=[jax.ShapeDtypeStruct((S, LANES), F32), jax.ShapeDtypeStruct((8, LANES), F32)],
        scratch_shapes=[pltpu.VMEM((1, LANES), F32)], compiler_params=_cparams(),
    )(small, bias_vec, alog_vec, dgates)


BQ = 256
NQ = S // BQ
FOX_SCALE = FHD ** -0.5


def _fox_logits(q, k, cum_q, cum_k, qi, kj):
    s = _dot(q, k, 1, 1) * FOX_SCALE + (cum_q - cum_k)
    row = qi * BQ + lax.broadcasted_iota(jnp.int32, (BQ, BQ), 0)
    col = kj * BQ + lax.broadcasted_iota(jnp.int32, (BQ, BQ), 1)
    return jnp.where(col <= row, s, -jnp.inf)


def _fox_fwd(q, k, v, cum_col, cum_row, w_norm):
    def body(q_ref, k_ref, v_ref, cc_ref, cr_ref, w_ref, o_ref, on_ref, lse_ref):
        qi = pl.program_id(1)
        qv = q_ref[...]
        cq = cc_ref[...]

        def step(kj, carry):
            m, l, acc = carry
            rows = pl.ds(pl.multiple_of(kj * BQ, BQ), BQ)
            s = _fox_logits(qv, k_ref[rows, :], cq, cr_ref[kj], qi, kj)
            m_new = jnp.maximum(m, jnp.max(s, axis=-1, keepdims=True))
            p = jnp.exp(s - m_new)
            alpha = jnp.exp(m - m_new)
            l = alpha * l + jnp.sum(p, axis=-1, keepdims=True)
            acc = alpha * acc + _dot(p, v_ref[rows, :])
            return m_new, l, acc

        init = (jnp.full((BQ, 1), -jnp.inf, F32), jnp.zeros((BQ, 1), F32), jnp.zeros((BQ, FHD), F32))
        m, l, acc = lax.fori_loop(0, qi + 1, step, init)
        o = acc / l
        o_ref[...] = o
        on_ref[...] = (o * _rms_scale(o) * w_ref[...]).astype(BF16)
        lse_ref[...] = m + jnp.log(l)

    blk = pl.BlockSpec((None, BQ, FHD), lambda h, i: (h, i, 0))
    full = pl.BlockSpec((None, S, FHD), lambda h, i: (h, 0, 0))
    col = pl.BlockSpec((None, BQ, 1), lambda h, i: (h, i, 0))
    return pl.pallas_call(
        body, name="fox_fwd", grid=(NFH, NQ),
        in_specs=[blk, full, full, col, pl.BlockSpec((None, NQ, 1, BQ), lambda h, i: (h, 0, 0, 0)),
                  pl.BlockSpec((1, FHD), lambda h, i: (0, 0))],
        out_specs=[blk, blk, col],
        out_shape=[jax.ShapeDtypeStruct((NFH, S, FHD), F32), jax.ShapeDtypeStruct((NFH, S, FHD), BF16),
                   jax.ShapeDtypeStruct((NFH, S, 1), F32)],
        compiler_params=_cparams(),
    )(q, k, v, cum_col, cum_row, w_norm.reshape(1, FHD))


def _fox_norm_bwd(o, don, w_norm):
    def body(o_ref, g_ref, w_ref, do_ref, dl_ref, dw_ref):
        first = (pl.program_id(0) == 0) & (pl.program_id(1) == 0)
        ov = o_ref[...]
        do, dwt = _rms_bwd(ov, w_ref[...], g_ref[...])
        do_ref[...] = do.astype(BF16)
        dl_ref[...] = jnp.sum(do * ov, axis=-1, keepdims=True)

        @pl.when(first)
        def _():
            dw_ref[...] = jnp.zeros_like(dw_ref)

        dw_ref[...] += jnp.sum(dwt, axis=0, keepdims=True)

    blk = pl.BlockSpec((None, BQ, FHD), lambda h, i: (h, i, 0))
    col = pl.BlockSpec((None, BQ, 1), lambda h, i: (h, i, 0))
    vec = pl.BlockSpec((1, FHD), lambda h, i: (0, 0))
    return pl.pallas_call(
        body, name="fox_norm_bwd", grid=(NFH, NQ), in_specs=[blk, blk, vec], out_specs=[blk, col, vec],
        out_shape=[jax.ShapeDtypeStruct((NFH, S, FHD), BF16), jax.ShapeDtypeStruct((NFH, S, 1), F32),
                   jax.ShapeDtypeStruct((1, FHD), F32)],
        compiler_params=_cparams(),
    )(o, don, w_norm.reshape(1, FHD))


def _fox_bwd(q, k, v, do, cum_col, cum_row, lse, delta):
    def body(q_ref, k_ref, v_ref, do_ref, cc_ref, cr_ref, lse_ref, dl_ref, dq_ref, dk_ref, dv_ref, cs_ref, rs_ref):
        kj = pl.program_id(1)

        @pl.when(kj == 0)
        def _():
            dq_ref[...] = jnp.zeros_like(dq_ref)
            rs_ref[...] = jnp.zeros_like(rs_ref)

        kv = k_ref[...]
        vv = v_ref[...]
        ck = cr_ref[...]

        def step(qi, carry):
            dk, dv, cs = carry
            rows = pl.ds(pl.multiple_of(qi * BQ, BQ), BQ)
            qv = q_ref[rows, :]
            dov = do_ref[rows, :]
            s = _fox_logits(qv, kv, cc_ref[rows, :], ck, qi, kj)
            p = jnp.exp(s - lse_ref[rows, :])
            dp = _dot(dov, vv, 1, 1)
            ds = p * (dp - dl_ref[rows, :])
            dv = dv + _dot(p, dov, 0, 0)
            dk = dk + _dot(ds, qv, 0, 0) * FOX_SCALE
            dq_ref[rows, :] += _dot(ds, kv) * FOX_SCALE
            cs = cs + jnp.sum(ds, axis=0, keepdims=True)
            rs_ref[rows, :] += jnp.sum(ds, axis=1, keepdims=True)
            return dk, dv, cs

        init = (jnp.zeros((BQ, FHD), F32), jnp.zeros((BQ, FHD), F32), jnp.zeros((1, BQ), F32))
        dk, dv, cs = lax.fori_loop(kj, NQ, step, init)
        dk_ref[...] = dk
        dv_ref[...] = dv
        cs_ref[...] = cs

    blk = pl.BlockSpec((None, BQ, FHD), lambda h, j: (h, j, 0))
    full = pl.BlockSpec((None, S, FHD), lambda h, j: (h, 0, 0))
    colfull = pl.BlockSpec((None, S, 1), lambda h, j: (h, 0, 0))
    rowblk = pl.BlockSpec((None, None, 1, BQ), lambda h, j: (h, j, 0, 0))
    return pl.pallas_call(
        body, name="fox_bwd", grid=(NFH, NQ),
        in_specs=[full, blk, blk, full, colfull, rowblk, colfull, colfull],
        out_specs=[full, blk, blk, rowblk, colfull],
        out_shape=[jax.ShapeDtypeStruct((NFH, S, FHD), F32), jax.ShapeDtypeStruct((NFH, S, FHD), F32),
                   jax.ShapeDtypeStruct((NFH, S, FHD), F32), jax.ShapeDtypeStruct((NFH, NQ, 1, BQ), F32),
                   jax.ShapeDtypeStruct((NFH, S, 1), F32)],
        compiler_params=_cparams(),
    )(q, k, v, do, cum_col, cum_row, lse, delta)


NQKV = 3 * NGH
GDN_QSCALE = GHD ** -0.5


def _shift_down(x, s):
    if s == 0:
        return x
    row = lax.broadcasted_iota(jnp.int32, x.shape, 0)
    return jnp.where(row >= s, pltpu.roll(x, s, 0), 0.0)


def _shift_up(x, s):
    if s == 0:
        return x
    n = x.shape[0]
    row = lax.broadcasted_iota(jnp.int32, x.shape, 0)
    return jnp.where(row < n - s, pltpu.roll(x, n - s, 0), 0.0)


def _conv_pre(xv, wv):
    pre = xv * wv[CONV_K - 1:CONV_K, :]
    for j in range(CONV_K - 1):
        pre = pre + _shift_down(xv, CONV_K - 1 - j) * wv[j:j + 1, :]
    return pre


def _l2_factors(b):
    return b < 2 * NGH, jnp.where(b < NGH, GDN_QSCALE, 1.0)


def _gdn_pre(xin, conv_w):
    def body(x_ref, w_ref, o_ref):
        b = pl.program_id(0)
        c = _silu(_conv_pre(x_ref[...], w_ref[...]))
        normed, scale = _l2_factors(b)
        rs = lax.rsqrt(jnp.sum(c * c, axis=-1, keepdims=True) + EPS)
        o_ref[...] = c * jnp.where(normed, rs, 1.0) * scale

    return pl.pallas_call(
        body, name="gdn_pre", grid=(NQKV,),
        in_specs=[pl.BlockSpec((S, GHD), lambda b: (0, b)), pl.BlockSpec((CONV_K, GHD), lambda b: (0, b))],
        out_specs=pl.BlockSpec((S, GHD), lambda b: (0, b)),
        out_shape=jax.ShapeDtypeStruct((S, NQKV * GHD), F32), compiler_params=_cparams(),
    )(xin, conv_w)


def _gdn_pre_bwd(xin, conv_w, dy):
    def body(x_ref, w_ref, dy_ref, dx_ref, dw_ref):
        b = pl.program_id(0)
        xv = x_ref[...]
        wv = w_ref[...]
        pre = _conv_pre(xv, wv)
        sig = _sigmoid(pre)
        c = pre * sig
        normed, scale = _l2_factors(b)
        g = dy_ref[...] * scale
        rs = lax.rsqrt(jnp.sum(c * c, axis=-1, keepdims=True) + EPS)
        dc_n = rs * g - c * (rs * rs * rs) * jnp.sum(g * c, axis=-1, keepdims=True)
        dc = jnp.where(normed, dc_n, g)
        dpre = dc * sig * (1.0 + pre * (1.0 - sig))
        dx = dpre * wv[CONV_K - 1:CONV_K, :]
        for j in range(CONV_K - 1):
            dx = dx + _shift_up(dpre, CONV_K - 1 - j) * wv[j:j + 1, :]
        dx_ref[...] = dx
        for j in range(CONV_K):
            dw_ref[j:j + 1, :] = jnp.sum(dpre * _shift_down(xv, CONV_K - 1 - j), axis=0, keepdims=True)

    return pl.pallas_call(
        body, name="gdn_pre_bwd", grid=(NQKV,),
        in_specs=[pl.BlockSpec((S, GHD), lambda b: (0, b)), pl.BlockSpec((CONV_K, GHD), lambda b: (0, b)),
                  pl.BlockSpec((S, GHD), lambda b: (0, b))],
        out_specs=[pl.BlockSpec((S, GHD), lambda b: (0, b)), pl.BlockSpec((CONV_K, GHD), lambda b: (0, b))],
        out_shape=[jax.ShapeDtypeStruct((S, NQKV * GHD), F32), jax.ShapeDtypeStruct((CONV_K, NQKV * GHD), F32)],
        compiler_params=_cparams(),
    )(xin, conv_w, dy)


CB = 4
NCB = NCH // CB


def _chunk_prep(q, k, v, gcol, bcol):
    r = lax.broadcasted_iota(jnp.int32, (CHUNK, CHUNK), 0)
    c = lax.broadcasted_iota(jnp.int32, (CHUNK, CHUNK), 1)
    incl = c <= r
    eye = (r == c).astype(F32)
    grow = jnp.sum(gcol * eye, axis=0, keepdims=True)
    gc_col = jnp.sum(jnp.where(incl, grow, 0.0), axis=1, keepdims=True)
    gc_row = jnp.sum(jnp.where(r <= c, gcol, 0.0), axis=0, keepdims=True)
    decay = jnp.exp(jnp.where(incl, gc_col - gc_row, -jnp.inf))
    kb = k * bcol
    vb = v * bcol
    m = jnp.where(c < r, _mm_nt(kb, k) * decay, 0.0)
    t_inv = eye - m
    p = _hm_nn(m, m)
    for step in range(5):
        t_inv = t_inv + _hm_nn(t_inv, p)
        if step < 4:
            p = _hm_nn(p, p)
    egc = jnp.exp(gc_col)
    u = _mm_nn(t_inv, vb)
    w = _mm_nn(t_inv, kb * egc)
    a_intra = _mm_nt(q, k) * decay
    gc_last = gc_col[CHUNK - 1:CHUNK, :]
    return u, w, a_intra, q * egc, k * jnp.exp(gc_last - gc_col), jnp.exp(gc_last)


def _gate_col(gates, lane_index):
    lane = lax.broadcasted_iota(jnp.int32, gates.shape, 1)
    return jnp.sum(jnp.where(lane == lane_index, gates, 0.0), axis=1, keepdims=True)


def _prep_specs():
    rows = CB * CHUNK
    qs = pl.BlockSpec((rows, GHD), lambda i, h: (i, h))
    ks = pl.BlockSpec((rows, GHD), lambda i, h: (i, NGH + h))
    vs = pl.BlockSpec((rows, GHD), lambda i, h: (i, 2 * NGH + h))
    gs = pl.BlockSpec((rows, LANES), lambda i, h: (i, 0))
    hs = pl.BlockSpec((rows, GHD), lambda i, h: (i, h))
    a_s = pl.BlockSpec((None, rows, CHUNK), lambda i, h: (h, i, 0))
    gl_s = pl.BlockSpec((None, CB, 1, LANES), lambda i, h: (h, i, 0, 0))
    return qs, ks, vs, gs, hs, a_s, gl_s


def _gdn_prep(qkv, gates):
    def body(q_ref, k_ref, v_ref, g_ref, u_ref, w_ref, qd_ref, kd_ref, a_ref, gl_ref):
        h = pl.program_id(1)
        for cidx in range(CB):
            rows = pl.ds(cidx * CHUNK, CHUNK)
            gt = g_ref[rows, :]
            u, w, a, qd, kd, gl = _chunk_prep(q_ref[rows, :], k_ref[rows, :], v_ref[rows, :],
                                              _gate_col(gt, LANE_G + h), _gate_col(gt, LANE_BETA + h))
            u_ref[rows, :] = u
            w_ref[rows, :] = w
            qd_ref[rows, :] = qd
            kd_ref[rows, :] = kd
            a_ref[rows, :] = a
            gl_ref[cidx] = jnp.broadcast_to(gl, (1, LANES))

    qs, ks, vs, gs, hs, a_s, gl_s = _prep_specs()
    tok = jax.ShapeDtypeStruct((S, DGDN), F32)
    return pl.pallas_call(
        body, name="gdn_prep", grid=(NCB, NGH), in_specs=[qs, ks, vs, gs], out_specs=[hs, hs, hs, hs, a_s, gl_s],
        out_shape=[tok, tok, tok, tok, jax.ShapeDtypeStruct((NGH, S, CHUNK), F32),
                   jax.ShapeDtypeStruct((NGH, NCH, 1, LANES), F32)],
        compiler_params=_cparams(),
    )(qkv, qkv, qkv, gates)


def _gdn_prep_bwd(qkv, gates, du, dw, dqd, dkd, da, dgl):
    def body(q_ref, k_ref, v_ref, g_ref, du_ref, dw_ref, dqd_ref, dkd_ref, da_ref, dgl_ref,
             dq_ref, dk_ref, dv_ref, dg_ref):
        h = pl.program_id(1)

        @pl.when(h == 0)
        def _():
            dg_ref[...] = jnp.zeros_like(dg_ref)

        lane = lax.broadcasted_iota(jnp.int32, (CHUNK, LANES), 1)
        for cidx in range(CB):
            rows = pl.ds(cidx * CHUNK, CHUNK)
            gt = g_ref[rows, :]
            args = (q_ref[rows, :], k_ref[rows, :], v_ref[rows, :], _gate_col(gt, LANE_G + h), _gate_col(gt, LANE_BETA + h))
            _, vjp = jax.vjp(_chunk_prep, *args)
            cts = (du_ref[rows, :], dw_ref[rows, :], da_ref[rows, :], dqd_ref[rows, :], dkd_ref[rows, :],
                   dgl_ref[cidx][:, 0:1])
            dq, dk, dv, dgc, dbc = vjp(cts)
            dq_ref[rows, :] = dq
            dk_ref[rows, :] = dk
            dv_ref[rows, :] = dv
            dg_ref[rows, :] += jnp.where(lane == LANE_G + h, dgc, 0.0) + jnp.where(lane == LANE_BETA + h, dbc, 0.0)

    qs, ks, vs, gs, hs, a_s, gl_s = _prep_specs()
    tok = jax.ShapeDtypeStruct((S, DGDN), F32)
    return pl.pallas_call(
        body, name="gdn_prep_bwd", grid=(NCB, NGH), in_specs=[qs, ks, vs, gs, hs, hs, hs, hs, a_s, gl_s],
        out_specs=[hs, hs, hs, gs], out_shape=[tok, tok, tok, jax.ShapeDtypeStruct((S, LANES), F32)],
        compiler_params=_cparams(),
    )(qkv, qkv, qkv, gates, du, dw, dqd, dkd, da, dgl)


def _scan_specs():
    hs = pl.BlockSpec((S, GHD), lambda h: (0, h))
    a_s = pl.BlockSpec((None, S, CHUNK), lambda h: (h, 0, 0))
    gl_s = pl.BlockSpec((None, NCH, 1, LANES), lambda h: (h, 0, 0, 0))
    st_s = pl.BlockSpec((None, NCH, GHD, GHD), lambda h: (h, 0, 0, 0))
    return hs, a_s, gl_s, st_s


def _gdn_scan(u, w, qd, kd, a, gl):
    def body(u_ref, w_ref, qd_ref, kd_ref, a_ref, gl_ref, o_ref, st_ref):
        def step(ci, state):
            rows = pl.ds(pl.multiple_of(ci * CHUNK, CHUNK), CHUNK)
            st_ref[ci] = state
            vn = u_ref[rows, :] - _dot(w_ref[rows, :], state)
            o_ref[rows, :] = _dot(qd_ref[rows, :], state) + _dot(a_ref[rows, :], vn)
            return state * gl_ref[ci] + _dot(kd_ref[rows, :], vn, 0, 0)

        lax.fori_loop(0, NCH, step, jnp.zeros((GHD, GHD), F32))

    hs, a_s, gl_s, st_s = _scan_specs()
    return pl.pallas_call(
        body, name="gdn_scan", grid=(NGH,), in_specs=[hs, hs, hs, hs, a_s, gl_s], out_specs=[hs, st_s],
        out_shape=[jax.ShapeDtypeStruct((S, DGDN), F32), jax.ShapeDtypeStruct((NGH, NCH, GHD, GHD), F32)],
        compiler_params=_cparams(),
    )(u, w, qd, kd, a, gl)


def _gdn_scan_bwd(do, u, w, qd, kd, a, gl, states):
    def body(do_ref, u_ref, w_ref, qd_ref, kd_ref, a_ref, gl_ref, st_ref,
             du_ref, dw_ref, dqd_ref, dkd_ref, da_ref, dgl_ref):
        def step(t, dstate):
            ci = NCH - 1 - t
            rows = pl.ds(pl.multiple_of(ci * CHUNK, CHUNK), CHUNK)
            state = st_ref[ci]
            dov = do_ref[rows, :]
            wv = w_ref[rows, :]
            kdv = kd_ref[rows, :]
            vn = u_ref[rows, :] - _dot(wv, state)
            dvn = _dot(a_ref[rows, :], dov, 0, 0) + _dot(kdv, dstate)
            da_ref[rows, :] = _dot(dov, vn, 1, 1)
            dqd_ref[rows, :] = _dot(dov, state, 1, 1)
            dkd_ref[rows, :] = _dot(vn, dstate, 1, 1)
            dgl = jnp.sum(jnp.sum(dstate * state, axis=1, keepdims=True), axis=0, keepdims=True)
            dgl_ref[ci] = jnp.broadcast_to(dgl, (1, LANES))
            du_ref[rows, :] = dvn
            dw_ref[rows, :] = -_dot(dvn, state, 1, 1)
            return dstate * gl_ref[ci] + _dot(qd_ref[rows, :], dov, 0, 0) - _dot(wv, dvn, 0, 0)

        lax.fori_loop(0, NCH, step, jnp.zeros((GHD, GHD), F32))

    hs, a_s, gl_s, st_s = _scan_specs()
    tok = jax.ShapeDtypeStruct((S, DGDN), F32)
    return pl.pallas_call(
        body, name="gdn_scan_bwd", grid=(NGH,), in_specs=[hs, hs, hs, hs, hs, a_s, gl_s, st_s],
        out_specs=[hs, hs, hs, hs, a_s, gl_s],
        out_shape=[tok, tok, tok, tok, jax.ShapeDtypeStruct((NGH, S, CHUNK), F32),
                   jax.ShapeDtypeStruct((NGH, NCH, 1, LANES), F32)],
        compiler_params=_cparams(),
    )(do, u, w, qd, kd, a, gl, states)


def _gdn_out(o, gz, w_norm):
    def body(o_ref, z_ref, w_ref, y_ref):
        ov = o_ref[...]
        y_ref[...] = (ov * _rms_scale(ov) * w_ref[...] * _silu(z_ref[...])).astype(BF16)

    blk = pl.BlockSpec((TR, GHD), lambda i, h: (i, h))
    return pl.pallas_call(
        body, name="gdn_out", grid=(S // TR, NGH), in_specs=[blk, blk, pl.BlockSpec((1, GHD), lambda i, h: (0, 0))],
        out_specs=blk, out_shape=jax.ShapeDtypeStruct((S, DGDN), BF16), compiler_params=_cparams(),
    )(o, gz, w_norm.reshape(1, GHD))


def _gdn_out_bwd(dy, o, gz, w_norm):
    def body(dy_ref, o_ref, z_ref, w_ref, do_ref, dz_ref, dw_ref):
        first = (pl.program_id(0) == 0) & (pl.program_id(1) == 0)
        ov = o_ref[...]
        zv = z_ref[...]
        w = w_ref[...]
        g = dy_ref[...]
        sig = _sigmoid(zv)
        dz_ref[...] = g * (ov * _rms_scale(ov) * w) * sig * (1.0 + zv * (1.0 - sig))
        do, dwt = _rms_bwd(ov, w, g * zv * sig)
        do_ref[...] = do

        @pl.when(first)
        def _():
            dw_ref[...] = jnp.zeros_like(dw_ref)

        dw_ref[...] += jnp.sum(dwt, axis=0, keepdims=True)

    blk = pl.BlockSpec((TR, GHD), lambda i, h: (i, h))
    vec = pl.BlockSpec((1, GHD), lambda i, h: (0, 0))
    tok = jax.ShapeDtypeStruct((S, DGDN), F32)
    return pl.pallas_call(
        body, name="gdn_out_bwd", grid=(S // TR, NGH), in_specs=[blk, blk, blk, vec], out_specs=[blk, blk, vec],
        out_shape=[tok, tok, jax.ShapeDtypeStruct((1, GHD), F32)], compiler_params=_cparams(),
    )(dy, o, gz, w_norm.reshape(1, GHD))


def _adamw(w, g, m, v, name):
    rows, cols = w.shape
    tr = rows if rows <= 256 else 256
    assert rows % tr == 0

    def body(w_ref, g_ref, m_ref, v_ref, d_ref, nm_ref, nv_ref):
        gv = g_ref[...]
        nm = ADAM_B1 * m_ref[...] + (1.0 - ADAM_B1) * gv
        nv = ADAM_B2 * v_ref[...] + (1.0 - ADAM_B2) * jnp.square(gv)
        m_hat = nm / (1.0 - ADAM_B1 ** ADAM_STEP)
        v_hat = nv / (1.0 - ADAM_B2 ** ADAM_STEP)
        d_ref[...] = -ADAM_LR * (m_hat / (jnp.sqrt(v_hat) + ADAM_EPS) + ADAM_WD * w_ref[...])
        nm_ref[...] = nm
        nv_ref[...] = nv

    spec = pl.BlockSpec((tr, cols), lambda i: (i, 0))
    shape = jax.ShapeDtypeStruct((rows, cols), F32)
    return pl.pallas_call(
        body, name=name, grid=(rows // tr,), in_specs=[spec] * 4, out_specs=[spec] * 3, out_shape=[shape] * 3,
        compiler_params=_cparams(),
    )(w, g, m, v)


def _place():
    return lax.axis_index("x"), lax.axis_index("y"), lax.axis_index("c")


def _other_chips(x, y):
    return [(1 - x, y), (x, 1 - y), (1 - x, 1 - y)]


HBM = pl.BlockSpec(memory_space=pltpu.HBM)


def _weights_allgather(packed):
    def body(src_ref, out_ref, send_sems, recv_sems, local_sem):
        x, y, c = _place()
        me_chip = 2 * x + y
        sibling = (x, y, 1 - c)
        chips = _other_chips(x, y)

        def copy(k, chip_index, half, to, src=None):
            dst = out_ref.at[chip_index, half]
            return pltpu.make_async_remote_copy(
                src_ref=dst if src is None else src, dst_ref=dst, send_sem=send_sems.at[k], recv_sem=recv_sems.at[k],
                device_id=to, device_id_type=MESH)

        mine = pltpu.make_async_copy(src_ref, out_ref.at[me_chip], local_sem)
        mine.start()
        first = [copy(j, me_chip, c, (*chip, c), src=src_ref.at[c]) for j, chip in enumerate(chips)]
        for cp in first:
            cp.start()
        passed = [copy(3 + j, 2 * chip[0] + chip[1], c, sibling) for j, chip in enumerate(chips)]
        for j, chip in enumerate(chips):
            copy(j, 2 * chip[0] + chip[1], c, (x, y, c)).wait_recv()
            passed[j].start()
        for j, chip in enumerate(chips):
            copy(3 + j, 2 * chip[0] + chip[1], 1 - c, (x, y, c)).wait_recv()
        for cp in first + passed:
            cp.wait_send()
        mine.wait()

    return pl.pallas_call(
        body, name="weights_allgather", in_specs=[HBM], out_specs=HBM,
        out_shape=jax.ShapeDtypeStruct((NCHIP, 2, ROWS_HALF, LANES), packed.dtype),
        scratch_shapes=[pltpu.SemaphoreType.DMA((6,)), pltpu.SemaphoreType.DMA((6,)), pltpu.SemaphoreType.DMA],
        compiler_params=_cparams(),
    )(packed)


def _pair_exchange(send, name):
    def body(src_ref, out_ref, send_sem, recv_sem):
        x, y, c = _place()
        cp = pltpu.make_async_remote_copy(src_ref=src_ref, dst_ref=out_ref, send_sem=send_sem, recv_sem=recv_sem,
                                          device_id=(x, y, 1 - c), device_id_type=MESH)
        cp.start()
        cp.wait()

    return pl.pallas_call(
        body, name=name, in_specs=[HBM], out_specs=HBM, out_shape=jax.ShapeDtypeStruct(send.shape, send.dtype),
        scratch_shapes=[pltpu.SemaphoreType.DMA, pltpu.SemaphoreType.DMA],
        compiler_params=_cparams(),
    )(send)


def _chip_exchange(parts):
    def body(src_ref, out_ref, send_sems, recv_sems):
        x, y, c = _place()
        copies = []
        for k, chip in enumerate(_other_chips(x, y)):
            cp = pltpu.make_async_remote_copy(
                src_ref=src_ref.at[2 * chip[0] + chip[1]], dst_ref=out_ref.at[k], send_sem=send_sems.at[k],
                recv_sem=recv_sems.at[k], device_id=(*chip, c), device_id_type=MESH)
            cp.start()
            copies.append(cp)
        for cp in copies:
            cp.wait()

    return pl.pallas_call(
        body, name="grads_chip_exchange", in_specs=[HBM], out_specs=HBM,
        out_shape=jax.ShapeDtypeStruct((NCHIP - 1,) + parts.shape[1:], parts.dtype),
        scratch_shapes=[pltpu.SemaphoreType.DMA((NCHIP - 1,)), pltpu.SemaphoreType.DMA((NCHIP - 1,))],
        compiler_params=_cparams(),
    )(parts)


def _allgather_small(vec, name):
    rows = vec.shape[0]

    def body(v_ref, out_ref, send_sems, recv_sems):
        x, y, c = _place()
        me = 4 * x + 2 * y + c
        out_ref[me] = v_ref[...]
        copies = []
        for k in range(1, NDEV):
            px, py, pc = x ^ ((k >> 2) & 1), y ^ ((k >> 1) & 1), c ^ (k & 1)
            cp = pltpu.make_async_remote_copy(
                src_ref=v_ref, dst_ref=out_ref.at[me], send_sem=send_sems.at[k - 1], recv_sem=recv_sems.at[k - 1],
                device_id=(px, py, pc), device_id_type=MESH)
            cp.start()
            copies.append(cp)
        for cp in copies:
            cp.wait()

    return pl.pallas_call(
        body, name=name, in_specs=[pl.BlockSpec(memory_space=pltpu.VMEM)], out_specs=pl.BlockSpec(memory_space=pltpu.VMEM),
        out_shape=jax.ShapeDtypeStruct((NDEV, rows, LANES), F32),
        scratch_shapes=[pltpu.SemaphoreType.DMA((NDEV - 1,)), pltpu.SemaphoreType.DMA((NDEV - 1,))],
        compiler_params=_cparams(),
    )(vec)


def _sum_leading(parts, name):
    n, rows, _ = parts.shape
    tr = rows if rows <= 1024 else TRP
    assert rows % tr == 0 and (tr % 8 == 0 or tr == rows)

    def body(p_ref, o_ref):
        acc = p_ref[0]
        for i in range(1, n):
            acc = acc + p_ref[i]
        o_ref[...] = acc

    return pl.pallas_call(
        body, name=name, grid=(rows // tr,), in_specs=[pl.BlockSpec((n, tr, LANES), lambda i: (0, i, 0))],
        out_specs=pl.BlockSpec((tr, LANES), lambda i: (i, 0)), out_shape=jax.ShapeDtypeStruct((rows, LANES), F32),
        compiler_params=_cparams(),
    )(parts)


def _add2(a, b, name):
    n, rows, _ = a.shape
    tr = TRP
    assert rows % tr == 0

    def body(a_ref, b_ref, o_ref):
        o_ref[...] = a_ref[...] + b_ref[...]

    spec = pl.BlockSpec((None, tr, LANES), lambda j, i: (j, i, 0))
    return pl.pallas_call(
        body, name=name, grid=(n, rows // tr), in_specs=[spec, spec], out_specs=spec,
        out_shape=jax.ShapeDtypeStruct(a.shape, F32), compiler_params=_cparams(),
    )(a, b)


def _pack_shard(w_in_s, w_out_s, w_up_s, w_down_s):
    halves = []
    for h in range(2):
        pieces = []
        for arr in (w_in_s, w_out_s, w_up_s, w_down_s):
            r = arr.shape[0] // 2
            pieces.append(arr[h * r:(h + 1) * r].reshape(-1, LANES))
        halves.append(jnp.concatenate(pieces, axis=0))
    return jnp.stack(halves)


def _unpack_half(half):
    o0, o1, o2 = ROWS_WIN, ROWS_WIN + ROWS_WOUT, ROWS_WIN + ROWS_WOUT + ROWS_WUP
    return (half[:o0].reshape(D // 2, DPROJ // NCHIP), half[o0:o1].reshape(D // NCHIP // 2, D),
            half[o1:o2].reshape(D // 2, DFF // NCHIP), half[o2:].reshape(DFF // NCHIP // 2, D))


def _unpack_shard(both):
    h0, h1 = _unpack_half(both[0]), _unpack_half(both[1])
    return tuple(jnp.concatenate([a, b], axis=0) for a, b in zip(h0, h1))


def _to_padded_cols(w):
    zeros = jnp.zeros((w.shape[0], DPROJ_PAD - COL_SMALL - 16), w.dtype)
    return jnp.concatenate([w[:, 0:1536], w[:, 1544:3080], w[:, 3088:3600], w[:, 1536:1544], w[:, 3080:3088], zeros], axis=1)


def _from_padded_cols(w):
    return jnp.concatenate([w[:, 0:1536], w[:, COL_SMALL:COL_SMALL + 8], w[:, 1536:3072],
                            w[:, COL_SMALL + 8:COL_SMALL + 16], w[:, COL_GZ:COL_GZ + 512]], axis=1)


def _heads_major(t):
    return t.reshape(S, NFH, FHD).transpose(1, 0, 2)


def _tokens_major(t):
    return t.transpose(1, 0, 2).reshape(S, DFOX)


def _local_step(x, target, win_p, wout, wup, wdown, pre_mix_norm, fox_f_bias, fox_out_norm, conv_w, gdn_a_log,
                gdn_dt_bias, gdn_out_norm, post_mix_norm, pre_mlp_norm, post_mlp_norm):
    bias_vec = jnp.zeros((1, LANES), F32).at[0, 0:NFH].set(fox_f_bias).at[0, LANE_G:LANE_G + NGH].set(gdn_dt_bias)
    alog_vec = jnp.zeros((1, LANES), F32).at[0, LANE_G:LANE_G + NGH].set(gdn_a_log)

    h = _pre_norm(x, pre_mix_norm)
    proj = _matmul(h, win_p, tn=768, tk=1024, name="mm_proj")
    small = proj[:, COL_SMALL:COL_SMALL + LANES]
    gates = _gates(small, bias_vec, alog_vec)
    q = _heads_major(proj[:, 0:DFOX]).astype(BF16)
    k = _heads_major(proj[:, DFOX:2 * DFOX]).astype(BF16)
    v = _heads_major(proj[:, 2 * DFOX:3 * DFOX]).astype(BF16)
    cum = gates[:, 0:NFH].T
    cum_col = cum.reshape(NFH, S, 1)
    cum_row = cum.reshape(NFH, NQ, 1, BQ)
    fox_o, fox_on, lse = _fox_fwd(q, k, v, cum_col, cum_row, fox_out_norm)

    xin = proj[:, 3 * DFOX:3 * DFOX + 3 * DGDN]
    gz = proj[:, COL_GZ:COL_GZ + DGDN]
    qkv = _gdn_pre(xin, conv_w)
    u, w, qd, kd, a_intra, gl = _gdn_prep(qkv, gates)
    gdn_raw, states = _gdn_scan(u, w, qd, kd, a_intra, gl)
    gdn_o = _gdn_out(gdn_raw, gz, gdn_out_norm)

    mix = jnp.concatenate([_tokens_major(fox_on), gdn_o], axis=1)
    mixed = _matmul(mix, wout, name="mm_out")
    x1, h2 = _post_mix(x, mixed, post_mix_norm, pre_mlp_norm)
    up = _matmul(h2, wup, tk=1024, name="mm_up")
    act = _relu2(up)
    y = _matmul(act, wdown, tk=1024, name="mm_down")
    dx2, dy, d_post_mlp, loss_row = _loss_head(x1, y, post_mlp_norm, target)

    dwdown = _matmul(act, dy, ta=True, tk=1024, name="mm_dwdown")
    dact = _matmul(dy, wdown, tb=True, tk=1024, name="mm_dact")
    dup = _relu2_bwd(dact, up)
    dwup = _matmul(h2, dup, ta=True, tk=1024, name="mm_dwup")
    dh2 = _matmul(dup, wup, tb=True, tk=1024, name="mm_dh2")
    dx1, dmixed, d_pre_mlp, d_post_mix = _mid_bwd(dh2, x1, pre_mlp_norm, dx2, mixed, post_mix_norm)
    dwout = _matmul(mix, dmixed, ta=True, tk=1024, name="mm_dwout")
    dmix = _matmul(dmixed, wout, tb=True, tk=1024, name="mm_dmix")

    dgdn_raw, dgz, d_gdn_norm = _gdn_out_bwd(dmix[:, DFOX:], gdn_raw, gz, gdn_out_norm)
    du, dw, dqd, dkd, da, dgl = _gdn_scan_bwd(dgdn_raw, u, w, qd, kd, a_intra, gl, states)
    dq_g, dk_g, dv_g, dgates_gdn = _gdn_prep_bwd(qkv, gates, du, dw, dqd, dkd, da, dgl)
    dxin, d_conv = _gdn_pre_bwd(xin, conv_w, jnp.concatenate([dq_g, dk_g, dv_g], axis=1))

    dfox_n = _heads_major(dmix[:, :DFOX])
    dfox, delta, d_fox_norm = _fox_norm_bwd(fox_o, dfox_n, fox_out_norm)
    dq, dk, dv, colsum, rowsum = _fox_bwd(q, k, v, dfox, cum_col, cum_row, lse, delta)
    dcum = (rowsum.reshape(NFH, S) - colsum.reshape(NFH, S)).T
    dgates = dgates_gdn.at[:, 0:NFH].set(dcum)
    dsmall, red = _gates_bwd(small, bias_vec, alog_vec, dgates)

    dproj = jnp.concatenate([_tokens_major(dq), _tokens_major(dk), _tokens_major(dv), dxin, dgz, dsmall,
                             jnp.zeros((S, DPROJ_PAD - COL_SMALL - LANES), F32)], axis=1).astype(BF16)
    dwin_p = _matmul(h, dproj, ta=True, tn=768, tk=1024, name="mm_dwin")
    dh = _matmul(dproj, win_p, tb=True, tk=768, name="mm_dh")
    grad_x, d_pre_mix = _pre_norm_bwd(dh, x, pre_mix_norm, dx1)

    small_grads = dict(
        pre_mix_norm=d_pre_mix[0], fox_f_bias=red[0, 0:NFH], fox_out_norm=d_fox_norm[0], gdn_conv_w=d_conv,
        gdn_a_log=red[1, LANE_G:LANE_G + NGH], gdn_dt_bias=red[0, LANE_G:LANE_G + NGH], gdn_out_norm=d_gdn_norm[0],
        post_mix_norm=d_post_mix[0], pre_mlp_norm=d_pre_mlp[0], post_mlp_norm=d_post_mlp[0])
    return loss_row[0, 0], grad_x, (dwin_p, dwout, dwup, dwdown), small_grads


SMALL_NAMES = ("pre_mix_norm", "fox_f_bias", "fox_out_norm", "gdn_conv_w", "gdn_a_log", "gdn_dt_bias", "gdn_out_norm",
               "post_mix_norm", "pre_mlp_norm", "post_mlp_norm")
SMALL_SHAPES = dict(pre_mix_norm=(D,), fox_f_bias=(NFH,), fox_out_norm=(FHD,), gdn_conv_w=(CONV_K, 3 * DGDN),
                    gdn_a_log=(NGH,), gdn_dt_bias=(NGH,), gdn_out_norm=(GHD,), post_mix_norm=(D,), pre_mlp_norm=(D,),
                    post_mlp_norm=(D,))
SMALL_ROWS = 88


def _pack_small(values):
    rows = []
    for name in SMALL_NAMES:
        flat = values[name].reshape(-1)
        pad = (-flat.shape[0]) % LANES
        rows.append(jnp.pad(flat, (0, pad)).reshape(-1, LANES))
    packed = jnp.concatenate(rows, axis=0)
    return jnp.pad(packed, ((0, SMALL_ROWS - packed.shape[0]), (0, 0)))


def _unpack_small(packed):
    out, row = {}, 0
    for name in SMALL_NAMES:
        shape = SMALL_SHAPES[name]
        size = 1
        for s in shape:
            size *= s
        nrows = -(-size // LANES)
        out[name] = packed[row:row + nrows].reshape(-1)[:size].reshape(shape)
        row += nrows
    return out


def kernel(x, pre_mix_norm, w_in, fox_f_bias, fox_out_norm, gdn_conv_w, gdn_a_log, gdn_dt_bias, gdn_out_norm, w_out, post_mix_norm, pre_mlp_norm, w_up, w_down, post_mlp_norm, loss_target, m_pre_mix_norm, m_w_in, m_fox_f_bias, m_fox_out_norm, m_gdn_conv_w, m_gdn_a_log, m_gdn_dt_bias, m_gdn_out_norm, m_w_out, m_post_mix_norm, m_pre_mlp_norm, m_w_up, m_w_down, m_post_mlp_norm, v_pre_mix_norm, v_w_in, v_fox_f_bias, v_fox_out_norm, v_gdn_conv_w, v_gdn_a_log, v_gdn_dt_bias, v_gdn_out_norm, v_w_out, v_post_mix_norm, v_pre_mlp_norm, v_w_up, v_w_down, v_post_mlp_norm):
    xi, yi, ci = _place()
    chip = 2 * xi + yi

    packed = _pack_shard(w_in.astype(BF16), w_out.astype(BF16), w_up.astype(BF16), w_down.astype(BF16))
    gathered = _weights_allgather(packed)
    shards = [_unpack_shard(gathered[j]) for j in range(NCHIP)]
    win_p = _to_padded_cols(jnp.concatenate([s[0] for s in shards], axis=1))
    wout = jnp.concatenate([s[1] for s in shards], axis=0)
    wup = jnp.concatenate([s[2] for s in shards], axis=1)
    wdown = jnp.concatenate([s[3] for s in shards], axis=0)
    conv_rows = CONV_K * (3 * DGDN // NCHIP) // LANES
    conv_all = _allgather_small(jnp.pad(gdn_conv_w.reshape(conv_rows, LANES), ((0, 16 - conv_rows), (0, 0))), "conv_allgather")
    conv_full = jnp.concatenate([conv_all[2 * j, :conv_rows].reshape(CONV_K, -1) for j in range(NCHIP)], axis=1)

    loss_local, grad_x, big, small = _local_step(
        x[0], loss_target[0], win_p, wout, wup, wdown, pre_mix_norm, fox_f_bias, fox_out_norm, conv_full, gdn_a_log,
        gdn_dt_bias, gdn_out_norm, post_mix_norm, pre_mlp_norm, post_mlp_norm)
    loss = lax.psum(loss_local, ("x", "y", "c"))

    dwin = _from_padded_cols(big[0])
    cw, rw, fw = DPROJ // NCHIP, D // NCHIP, DFF // NCHIP
    g_packed = jnp.stack([_pack_shard(dwin[:, j * cw:(j + 1) * cw], big[1][j * rw:(j + 1) * rw], big[2][:, j * fw:(j + 1) * fw],
                                      big[3][j * fw:(j + 1) * fw]) for j in range(NCHIP)])
    mine_half = lax.dynamic_index_in_dim(g_packed, ci, axis=1, keepdims=False)
    other_half = lax.dynamic_index_in_dim(g_packed, 1 - ci, axis=1, keepdims=False)
    from_sibling = _pair_exchange(other_half, "grads_pair_exchange")
    pair_sum = _add2(mine_half, from_sibling, "grads_pair_sum")
    from_chips = _chip_exchange(pair_sum)
    own = lax.dynamic_index_in_dim(pair_sum, chip, axis=0, keepdims=True)
    reduced_half = _sum_leading(jnp.concatenate([own, from_chips], axis=0), "grads_chip_sum")
    sibling_half = _pair_exchange(reduced_half, "grads_pair_share")
    both = jnp.where(ci == 0, jnp.stack([reduced_half, sibling_half]), jnp.stack([sibling_half, reduced_half]))
    g_win, g_wout, g_wup, g_wdown = _unpack_shard(both)

    small_all = _allgather_small(_pack_small(small), "small_allgather")
    g_small = _unpack_small(_sum_leading(small_all, "small_sum"))
    conv_cols = 3 * DGDN // NCHIP
    g_small["gdn_conv_w"] = lax.dynamic_slice_in_dim(g_small["gdn_conv_w"], chip * conv_cols, conv_cols, axis=1)

    grads = dict(g_small, w_in=g_win, w_out=g_wout, w_up=g_wup, w_down=g_wdown)
    weights = dict(pre_mix_norm=pre_mix_norm, w_in=w_in, fox_f_bias=fox_f_bias, fox_out_norm=fox_out_norm, gdn_conv_w=gdn_conv_w,
                   gdn_a_log=gdn_a_log, gdn_dt_bias=gdn_dt_bias, gdn_out_norm=gdn_out_norm, w_out=w_out, post_mix_norm=post_mix_norm,
                   pre_mlp_norm=pre_mlp_norm, w_up=w_up, w_down=w_down, post_mlp_norm=post_mlp_norm)
    m_in = dict(pre_mix_norm=m_pre_mix_norm, w_in=m_w_in, fox_f_bias=m_fox_f_bias, fox_out_norm=m_fox_out_norm, gdn_conv_w=m_gdn_conv_w,
                gdn_a_log=m_gdn_a_log, gdn_dt_bias=m_gdn_dt_bias, gdn_out_norm=m_gdn_out_norm, w_out=m_w_out, post_mix_norm=m_post_mix_norm,
                pre_mlp_norm=m_pre_mlp_norm, w_up=m_w_up, w_down=m_w_down, post_mlp_norm=m_post_mlp_norm)
    v_in = dict(pre_mix_norm=v_pre_mix_norm, w_in=v_w_in, fox_f_bias=v_fox_f_bias, fox_out_norm=v_fox_out_norm, gdn_conv_w=v_gdn_conv_w,
                gdn_a_log=v_gdn_a_log, gdn_dt_bias=v_gdn_dt_bias, gdn_out_norm=v_gdn_out_norm, w_out=v_w_out, post_mix_norm=v_post_mix_norm,
                pre_mlp_norm=v_pre_mlp_norm, w_up=v_w_up, w_down=v_w_down, post_mlp_norm=v_post_mlp_norm)
    order_w = ("pre_mix_norm", "w_in", "fox_f_bias", "fox_out_norm", "gdn_conv_w", "gdn_a_log", "gdn_dt_bias", "gdn_out_norm", "w_out",
               "post_mix_norm", "pre_mlp_norm", "w_up", "w_down", "post_mlp_norm")
    delta, new_m, new_v = {}, {}, {}
    for name in ("w_in", "w_out", "w_up", "w_down"):
        delta[name], new_m[name], new_v[name] = _adamw(weights[name], grads[name], m_in[name], v_in[name], "adamw_" + name)

    def small_local(values):
        return _pack_small({n: values[n] for n in SMALL_NAMES})

    local_shapes = dict(SMALL_SHAPES, gdn_conv_w=(CONV_K, conv_cols))
    ds, nms, nvs = _adamw(small_local(weights), small_local(grads), small_local(m_in), small_local(v_in), "adamw_small")

    def unpack_local(packed_rows):
        out, row = {}, 0
        for n in SMALL_NAMES:
            shape = local_shapes[n]
            size = 1
            for s in shape:
                size *= s
            nrows = -(-size // LANES)
            out[n] = packed_rows[row:row + nrows].reshape(-1)[:size].reshape(shape)
            row += nrows
        return out

    delta.update(unpack_local(ds))
    new_m.update(unpack_local(nms))
    new_v.update(unpack_local(nvs))
    return (loss, grad_x[None], *[grads[n] for n in order_w], *[delta[n] for n in order_w], *[new_m[n] for n in order_w],
            *[new_v[n] for n in order_w])
```

```python
import jax
import jax.numpy as jnp
from jax import lax
from jax.experimental import pallas as pl
from jax.experimental.pallas import tpu as pltpu

F32 = jnp.float32
BF16 = jnp.bfloat16
MESH = pl.DeviceIdType.MESH

S = 2048
D = 1024
NFH, FHD = 8, 64
NPAIR = NFH // 2
NGH, GHD = 4, 128
DFOX = NFH * FHD
DGDN = NGH * GHD
CHUNK = 64
NCH = S // CHUNK
CONV_K = 4
DFF = 4 * D
EPS = 1e-6
DPROJ = 3600
LANES = 128
DPROJ_PAD = 3840
BLK_GDN = 12
BLK_GZ = 24
BLK_SMALL = 28
NCHIP = 4
NDEV = 8
VMEM_LIMIT = 56 * 1024 * 1024

ADAM_LR = 0.001
ADAM_B1 = 0.9
ADAM_B2 = 0.999
ADAM_EPS = 1e-08
ADAM_WD = 0.01
ADAM_STEP = 10


def _cparams(**kw):
    return pltpu.CompilerParams(vmem_limit_bytes=VMEM_LIMIT, **kw)


def _dn(ca, cb):
    return (((ca,), (cb,)), ((), ()))


def _dot(a, b, ca=1, cb=0):
    return lax.dot_general(a.astype(BF16), b.astype(BF16), _dn(ca, cb), preferred_element_type=F32)


def _hdot(a, b, ca=1, cb=0):
    return lax.dot_general(a.astype(F32), b.astype(F32), _dn(ca, cb), precision=lax.Precision.HIGHEST,
                           preferred_element_type=F32)


@jax.custom_vjp
def _mm_nn(a, b):
    return _dot(a, b, 1, 0)


def _mm_nn_fwd(a, b):
    return _dot(a, b, 1, 0), (a, b)


def _mm_nn_bwd(res, g):
    a, b = res
    return _dot(g, b, 1, 1), _dot(a, g, 0, 0)


_mm_nn.defvjp(_mm_nn_fwd, _mm_nn_bwd)


@jax.custom_vjp
def _mm_nt(a, b):
    return _dot(a, b, 1, 1)


def _mm_nt_fwd(a, b):
    return _dot(a, b, 1, 1), (a, b)


def _mm_nt_bwd(res, g):
    a, b = res
    return _dot(g, b, 1, 0), _dot(g, a, 0, 0)


_mm_nt.defvjp(_mm_nt_fwd, _mm_nt_bwd)


@jax.custom_vjp
def _saved_inverse(m, t_inv):
    del m
    return t_inv


def _saved_inverse_fwd(m, t_inv):
    del m
    return t_inv, t_inv


def _saved_inverse_bwd(t_inv, g):
    return -_hdot(_hdot(t_inv, g, 0, 0), t_inv, 1, 1), jnp.zeros_like(t_inv)


_saved_inverse.defvjp(_saved_inverse_fwd, _saved_inverse_bwd)


def _sigmoid(z):
    return 1.0 / (1.0 + jnp.exp(-z))


def _softplus(z):
    return jnp.maximum(z, 0.0) + jnp.log(1.0 + jnp.exp(-jnp.abs(z)))


def _silu(z):
    return z * _sigmoid(z)


def _rms_scale(x):
    return lax.rsqrt(jnp.mean(x * x, axis=-1, keepdims=True) + EPS)


def _rms_bwd(x, w, g):
    r = _rms_scale(x)
    gw = g * w
    dx = r * gw - x * (r * r * r) * jnp.mean(gw * x, axis=-1, keepdims=True)
    return dx, g * x * r


def _matmul(a, b, *, name, ta=False, tb=False, tm=512, tn=512, tk=512, out_dtypes=(F32,), b3=False, o3=False,
            extra=(), epilogue=None):
    m, k = (a.shape[1], a.shape[0]) if ta else a.shape
    if b3:
        n = b.shape[1] if tb else b.shape[0] * b.shape[2]
        kb = b.shape[0] * b.shape[2] if tb else b.shape[1]
    else:
        n, kb = (b.shape[0], b.shape[1]) if tb else (b.shape[1], b.shape[0])
    assert kb == k, (name, kb, k)
    tm, tn, tk = min(tm, m), min(tn, n), min(tk, k)
    assert m % tm == 0 and n % tn == 0 and k % tk == 0, (name, m, n, k, tm, tn, tk)
    nk = k // tk
    n_extra = len(extra)
    n_out = len(out_dtypes)

    def body(*refs):
        a_ref, b_ref = refs[0], refs[1]
        extra_refs = refs[2:2 + n_extra]
        out_refs = refs[2 + n_extra:2 + n_extra + n_out]
        acc_ref = refs[-1]
        kk = pl.program_id(2)

        @pl.when(kk == 0)
        def _():
            acc_ref[...] = jnp.zeros_like(acc_ref)

        acc_ref[...] += _dot(a_ref[...], b_ref[...], 0 if ta else 1, 1 if tb else 0)

        @pl.when(kk == nk - 1)
        def _():
            acc = acc_ref[...]
            outs = (acc,) if epilogue is None else epilogue(acc, *[r[...] for r in extra_refs])
            for o_ref, val in zip(out_refs, outs):
                o_ref[...] = val.astype(o_ref.dtype)

    a_spec = pl.BlockSpec((tk, tm), lambda i, j, kk: (kk, i)) if ta else pl.BlockSpec((tm, tk), lambda i, j, kk: (i, kk))
    if b3 and tb:
        assert b.shape[2] == tk
        b_spec = pl.BlockSpec((None, tn, tk), lambda i, j, kk: (kk, j, 0))
    elif b3:
        assert b.shape[2] == tn
        b_spec = pl.BlockSpec((None, tk, tn), lambda i, j, kk: (j, kk, 0))
    elif tb:
        b_spec = pl.BlockSpec((tn, tk), lambda i, j, kk: (j, kk))
    else:
        b_spec = pl.BlockSpec((tk, tn), lambda i, j, kk: (kk, j))
    tile = pl.BlockSpec((tm, tn), lambda i, j, kk: (i, j))
    out_specs = [tile] * n_out
    out_shape = [jax.ShapeDtypeStruct((m, n), dt) for dt in out_dtypes]
    if o3:
        out_specs[0] = pl.BlockSpec((None, tm, tn), lambda i, j, kk: (j, i, 0))
        out_shape[0] = jax.ShapeDtypeStruct((n // tn, m, tn), out_dtypes[0])
    res = pl.pallas_call(
        body, name=name, grid=(m // tm, n // tn, nk),
        in_specs=[a_spec, b_spec] + [tile] * n_extra, out_specs=out_specs, out_shape=out_shape,
        scratch_shapes=[pltpu.VMEM((tm, tn), F32)],
        compiler_params=_cparams(dimension_semantics=("parallel", "parallel", "arbitrary")),
    )(a, b, *extra)
    return res[0] if n_out == 1 else res


TR = 256


def _row_spec(cols):
    return pl.BlockSpec((TR, cols), lambda i: (i, 0))


def _vec_spec(cols):
    return pl.BlockSpec((1, cols), lambda i: (0, 0))


def _pre_norm(x, w):
    def body(x_ref, w_ref, h_ref):
        xv = x_ref[...]
        h_ref[...] = (xv * _rms_scale(xv) * w_ref[...]).astype(BF16)

    return pl.pallas_call(
        body, name="pre_norm", grid=(S // TR,), in_specs=[_row_spec(D), _vec_spec(D)], out_specs=_row_spec(D),
        out_shape=jax.ShapeDtypeStruct((S, D), BF16), compiler_params=_cparams(),
    )(x, w)


def _post_mix(x, mixed, w_post, w_pre_mlp):
    def body(x_ref, m_ref, wp_ref, wm_ref, x1_ref, h2_ref):
        mv = m_ref[...]
        x1 = x_ref[...] + mv * _rms_scale(mv) * wp_ref[...]
        x1_ref[...] = x1
        h2_ref[...] = (x1 * _rms_scale(x1) * wm_ref[...]).astype(BF16)

    return pl.pallas_call(
        body, name="post_mix", grid=(S // TR,),
        in_specs=[_row_spec(D), _row_spec(D), _vec_spec(D), _vec_spec(D)], out_specs=[_row_spec(D), _row_spec(D)],
        out_shape=[jax.ShapeDtypeStruct((S, D), F32), jax.ShapeDtypeStruct((S, D), BF16)], compiler_params=_cparams(),
    )(x, mixed, w_post, w_pre_mlp)


def _loss_head(x1, y, w_post_mlp, target):
    def body(x1_ref, y_ref, w_ref, t_ref, dx2_ref, dy_ref, dw_ref, loss_ref):
        i = pl.program_id(0)
        yv = y_ref[...]
        w = w_ref[...]
        x2 = x1_ref[...] + yv * _rms_scale(yv) * w
        err = x2 - t_ref[...]
        dx2 = err * (1.0 / D)
        dx2_ref[...] = dx2
        dy, dwt = _rms_bwd(yv, w, dx2)
        dy_ref[...] = dy.astype(BF16)

        @pl.when(i == 0)
        def _():
            dw_ref[...] = jnp.zeros_like(dw_ref)
            loss_ref[...] = jnp.zeros_like(loss_ref)

        dw_ref[...] += jnp.sum(dwt, axis=0, keepdims=True)
        part = 0.5 * jnp.sum(jnp.mean(err * err, axis=-1, keepdims=True), axis=0, keepdims=True)
        loss_ref[...] += jnp.broadcast_to(part, loss_ref.shape)

    return pl.pallas_call(
        body, name="loss_head", grid=(S // TR,),
        in_specs=[_row_spec(D), _row_spec(D), _vec_spec(D), _row_spec(D)],
        out_specs=[_row_spec(D), _row_spec(D), _vec_spec(D), _vec_spec(LANES)],
        out_shape=[jax.ShapeDtypeStruct((S, D), F32), jax.ShapeDtypeStruct((S, D), BF16),
                   jax.ShapeDtypeStruct((1, D), F32), jax.ShapeDtypeStruct((1, LANES), F32)],
        compiler_params=_cparams(),
    )(x1, y, w_post_mlp, target)


def _mid_bwd(dh2, x1, w_pre_mlp, dx2, mixed, w_post):
    def body(dh2_ref, x1_ref, wm_ref, dx2_ref, m_ref, wp_ref, dx1_ref, dm_ref, dwm_ref, dwp_ref):
        i = pl.program_id(0)
        dxa, dwm = _rms_bwd(x1_ref[...], wm_ref[...], dh2_ref[...])
        dx1 = dx2_ref[...] + dxa
        dx1_ref[...] = dx1
        dm, dwp = _rms_bwd(m_ref[...], wp_ref[...], dx1)
        dm_ref[...] = dm.astype(BF16)

        @pl.when(i == 0)
        def _():
            dwm_ref[...] = jnp.zeros_like(dwm_ref)
            dwp_ref[...] = jnp.zeros_like(dwp_ref)

        dwm_ref[...] += jnp.sum(dwm, axis=0, keepdims=True)
        dwp_ref[...] += jnp.sum(dwp, axis=0, keepdims=True)

    return pl.pallas_call(
        body, name="mid_bwd", grid=(S // TR,),
        in_specs=[_row_spec(D), _row_spec(D), _vec_spec(D), _row_spec(D), _row_spec(D), _vec_spec(D)],
        out_specs=[_row_spec(D), _row_spec(D), _vec_spec(D), _vec_spec(D)],
        out_shape=[jax.ShapeDtypeStruct((S, D), F32), jax.ShapeDtypeStruct((S, D), BF16),
                   jax.ShapeDtypeStruct((1, D), F32), jax.ShapeDtypeStruct((1, D), F32)],
        compiler_params=_cparams(),
    )(dh2, x1, w_pre_mlp, dx2, mixed, w_post)


def _pre_norm_bwd(dh, x, w, dx1):
    def body(dh_ref, x_ref, w_ref, dx1_ref, dx_ref, dw_ref):
        i = pl.program_id(0)
        dxa, dwt = _rms_bwd(x_ref[...], w_ref[...], dh_ref[...])
        dx_ref[...] = dx1_ref[...] + dxa

        @pl.when(i == 0)
        def _():
            dw_ref[...] = jnp.zeros_like(dw_ref)

        dw_ref[...] += jnp.sum(dwt, axis=0, keepdims=True)

    return pl.pallas_call(
        body, name="pre_norm_bwd", grid=(S // TR,),
        in_specs=[_row_spec(D), _row_spec(D), _vec_spec(D), _row_spec(D)], out_specs=[_row_spec(D), _vec_spec(D)],
        out_shape=[jax.ShapeDtypeStruct((S, D), F32), jax.ShapeDtypeStruct((1, D), F32)], compiler_params=_cparams(),
    )(dh, x, w, dx1)


BQ = 256
NQ = S // BQ
LANE_BETA, LANE_G = 8, 12


def _gate_lanes(shape):
    lane = lax.broadcasted_iota(jnp.int32, shape, 1)
    return lane < LANE_BETA, (lane >= LANE_BETA) & (lane < LANE_G), (lane >= LANE_G) & (lane < LANE_G + NGH)


def _gates(proj, bias_vec, alog_vec):
    def body(s_ref, b_ref, a_ref, o_ref, t_ref, carry_ref):
        i = pl.program_id(0)

        @pl.when(i == 0)
        def _():
            carry_ref[...] = jnp.zeros_like(carry_ref)

        z = s_ref[...] + b_ref[...]
        tail = jnp.log(1.0 + jnp.exp(-jnp.abs(z)))
        sp = jnp.maximum(z, 0.0) + tail
        lf = jnp.minimum(z, 0.0) - tail
        r = lax.broadcasted_iota(jnp.int32, (BQ, BQ), 0)
        c = lax.broadcasted_iota(jnp.int32, (BQ, BQ), 1)
        tri = (c <= r).astype(F32)
        cum = _hdot(tri, lf) + carry_ref[...]
        carry_ref[...] = cum[BQ - 1:BQ, :]
        is_fox, is_beta, is_g = _gate_lanes(z.shape)
        out = jnp.where(is_fox, cum, jnp.where(is_beta, _sigmoid(z), jnp.where(is_g, -jnp.exp(a_ref[...]) * sp, 0.0)))
        o_ref[...] = out
        t_ref[...] = out.T

    return pl.pallas_call(
        body, name="gates", grid=(NQ,),
        in_specs=[pl.BlockSpec((BQ, LANES), lambda i: (i, BLK_SMALL)), _vec_spec(LANES), _vec_spec(LANES)],
        out_specs=[pl.BlockSpec((BQ, LANES), lambda i: (i, 0)), pl.BlockSpec((None, LANES, BQ), lambda i: (i, 0, 0))],
        out_shape=[jax.ShapeDtypeStruct((S, LANES), F32), jax.ShapeDtypeStruct((NQ, LANES, BQ), F32)],
        scratch_shapes=[pltpu.VMEM((1, LANES), F32)], compiler_params=_cparams(),
    )(proj, bias_vec, alog_vec)


def _gates_bwd(proj, bias_vec, alog_vec, dgates_gdn, dcum_fox, dproj):
    def body(s_ref, b_ref, a_ref, dg_ref, dc_ref, dproj_in, dproj_ref, red_ref, carry_ref):
        del dproj_in
        i = pl.program_id(0)

        @pl.when(i == 0)
        def _():
            carry_ref[...] = jnp.zeros_like(carry_ref)
            red_ref[...] = jnp.zeros_like(red_ref)

        z = s_ref[...] + b_ref[...]
        dg = dg_ref[...] + dc_ref[...]
        r = lax.broadcasted_iota(jnp.int32, (BQ, BQ), 0)
        c = lax.broadcasted_iota(jnp.int32, (BQ, BQ), 1)
        upper = (c >= r).astype(F32)
        dlf = _hdot(upper, dg) + carry_ref[...]
        carry_ref[...] = dlf[0:1, :]
        sig = _sigmoid(z)
        g_scale = -jnp.exp(a_ref[...])
        is_fox, is_beta, is_g = _gate_lanes(z.shape)
        ds = jnp.where(is_fox, dlf * (1.0 - sig), jnp.where(is_beta, dg * sig * (1.0 - sig), jnp.where(is_g, dg * g_scale * sig, 0.0)))
        dproj_ref[:, 0:LANES] = ds.astype(BF16)
        dproj_ref[:, LANES:2 * LANES] = jnp.zeros((BQ, LANES), BF16)
        dalog = jnp.where(is_g, dg * g_scale * _softplus(z), 0.0)
        sums = jnp.sum(ds, axis=0, keepdims=True)
        red_ref[0:1, :] += jnp.where(is_fox[0:1], sums, 0.0)
        red_ref[1:2, :] += pltpu.roll(jnp.where(is_g[0:1], sums, 0.0), LANES - LANE_G, 1)
        red_ref[2:3, :] += pltpu.roll(jnp.sum(dalog, axis=0, keepdims=True), LANES - LANE_G, 1)

    blk = pl.BlockSpec((BQ, LANES), lambda i: (NQ - 1 - i, 0))
    return pl.pallas_call(
        body, name="gates_bwd", grid=(NQ,),
        in_specs=[pl.BlockSpec((BQ, LANES), lambda i: (NQ - 1 - i, BLK_SMALL)), _vec_spec(LANES), _vec_spec(LANES), blk, blk,
                  pl.BlockSpec(memory_space=pl.ANY)],
        out_specs=[pl.BlockSpec((BQ, 2 * LANES), lambda i: (NQ - 1 - i, BLK_SMALL // 2)), pl.BlockSpec((8, LANES), lambda i: (0, 0))],
        out_shape=[jax.ShapeDtypeStruct((S, DPROJ_PAD), BF16), jax.ShapeDtypeStruct((8, LANES), F32)],
        input_output_aliases={5: 0},
        scratch_shapes=[pltpu.VMEM((1, LANES), F32)], compiler_params=_cparams(),
    )(proj, bias_vec, alog_vec, dgates_gdn, dcum_fox, dproj)


FOX_SCALE = FHD ** -0.5


def _head_mask(e):
    lane = lax.broadcasted_iota(jnp.int32, (1, LANES), 1)
    return (lane >= e * FHD) & (lane < (e + 1) * FHD)


def _lane_col(vals, index):
    lane = lax.broadcasted_iota(jnp.int32, vals.shape, 1)
    return jnp.sum(jnp.where(lane == index, vals, 0.0), axis=1, keepdims=True)


def _sublane_row(vals, index):
    row = lax.broadcasted_iota(jnp.int32, vals.shape, 0)
    return jnp.sum(jnp.where(row == index, vals, 0.0), axis=0, keepdims=True)


def _pair_cols(c0, c1):
    lane = lax.broadcasted_iota(jnp.int32, (c0.shape[0], 2), 1)
    return jnp.where(lane == 0, c0, c1)


def _fox_logits(q, k, cum_q, cum_k, qi, kj):
    s = _dot(q, k, 1, 1) * FOX_SCALE + (cum_q - cum_k)
    row = qi * BQ + lax.broadcasted_iota(jnp.int32, (BQ, BQ), 0)
    col = kj * BQ + lax.broadcasted_iota(jnp.int32, (BQ, BQ), 1)
    return jnp.where(col <= row, s, -jnp.inf)


def _head_rms(o, masks):
    o2 = o * o
    r = [lax.rsqrt(jnp.sum(jnp.where(mk, o2, 0.0), axis=1, keepdims=True) * (1.0 / FHD) + EPS) for mk in masks]
    return jnp.where(masks[0], r[0], r[1])


def _fox_fwd(proj, gates, gates_t, w2):
    def body(q_ref, k_ref, v_ref, g_ref, ct_ref, w_ref, mix_ref, o_ref, lse_ref, kb_ref, vb_ref):
        hp, qi = pl.program_id(0), pl.program_id(1)

        @pl.when(qi == 0)
        def _():
            kb_ref[...] = k_ref[...].astype(BF16)
            vb_ref[...] = v_ref[...].astype(BF16)

        masks = [_head_mask(0), _head_mask(1)]
        qblk = q_ref[...]
        gt = g_ref[...]
        qs = [jnp.where(mk, qblk, 0.0).astype(BF16) for mk in masks]
        cqs = [_lane_col(gt, 2 * hp + e) for e in range(2)]

        def step(kj, carry):
            rows = pl.ds(pl.multiple_of(kj * BQ, BQ), BQ)
            kb = kb_ref[rows, :]
            vb = vb_ref[rows, :]
            ctk = ct_ref[kj]
            new = []
            for e in range(2):
                m, l, acc = carry[e]
                s = _fox_logits(qs[e], kb, cqs[e], _sublane_row(ctk, 2 * hp + e), qi, kj)
                m_new = jnp.maximum(m, jnp.max(s, axis=-1, keepdims=True))
                p = jnp.exp(s - m_new)
                alpha = jnp.exp(m - m_new)
                new.append((m_new, alpha * l + jnp.sum(p, axis=-1, keepdims=True), alpha * acc + _dot(p, vb)))
            return tuple(new)

        one = (jnp.full((BQ, 1), -jnp.inf, F32), jnp.zeros((BQ, 1), F32), jnp.zeros((BQ, LANES), F32))
        (m0, l0, a0), (m1, l1, a1) = lax.fori_loop(0, qi + 1, step, (one, one))
        o = jnp.where(masks[0], a0 / l0, a1 / l1)
        o_ref[...] = o
        mix_ref[...] = (o * _head_rms(o, masks) * w_ref[...]).astype(BF16)
        lse_ref[...] = _pair_cols(m0 + jnp.log(l0), m1 + jnp.log(l1))

    def col(part):
        return pl.BlockSpec((S, LANES), lambda hp, i: (0, 3 * hp + part))

    blk = pl.BlockSpec((BQ, LANES), lambda hp, i: (i, hp))
    return pl.pallas_call(
        body, name="fox_fwd", grid=(NPAIR, NQ),
        in_specs=[pl.BlockSpec((BQ, LANES), lambda hp, i: (i, 3 * hp)), col(1), col(2),
                  pl.BlockSpec((BQ, LANES), lambda hp, i: (i, 0)), pl.BlockSpec((NQ, 8, BQ), lambda hp, i: (0, 0, 0)),
                  pl.BlockSpec((1, LANES), lambda hp, i: (0, 0))],
        out_specs=[blk, blk, pl.BlockSpec((None, BQ, 2), lambda hp, i: (hp, i, 0))],
        out_shape=[jax.ShapeDtypeStruct((S, D), BF16), jax.ShapeDtypeStruct((S, DFOX), F32),
                   jax.ShapeDtypeStruct((NPAIR, S, 2), F32)],
        scratch_shapes=[pltpu.VMEM((S, LANES), BF16), pltpu.VMEM((S, LANES), BF16)],
        compiler_params=_cparams(),
    )(proj, proj, proj, gates, gates_t, w2)


def _fox_norm_bwd(o, dmix, w2):
    def body(o_ref, g_ref, w_ref, do_ref, dl_ref, dw_ref):
        hp, qi = pl.program_id(0), pl.program_id(1)
        masks = [_head_mask(0), _head_mask(1)]
        ov = o_ref[...]
        g = g_ref[...]
        r = _head_rms(ov, masks)
        gw = g * w_ref[...]
        gwo = gw * ov
        mean = [jnp.sum(jnp.where(mk, gwo, 0.0), axis=1, keepdims=True) * (1.0 / FHD) for mk in masks]
        do = r * gw - ov * (r * r * r) * jnp.where(masks[0], mean[0], mean[1])
        do_ref[...] = do.astype(BF16)
        doo = do * ov
        dl_ref[...] = _pair_cols(*[jnp.sum(jnp.where(mk, doo, 0.0), axis=1, keepdims=True) for mk in masks])

        @pl.when((hp == 0) & (qi == 0))
        def _():
            dw_ref[...] = jnp.zeros_like(dw_ref)

        dw_ref[...] += jnp.sum(g * ov * r, axis=0, keepdims=True)

        @pl.when((hp == NPAIR - 1) & (qi == NQ - 1))
        def _():
            dw = dw_ref[...]
            dw_ref[...] = dw + pltpu.roll(dw, FHD, 1)

    blk = pl.BlockSpec((BQ, LANES), lambda hp, i: (i, hp))
    vec = pl.BlockSpec((1, LANES), lambda hp, i: (0, 0))
    return pl.pallas_call(
        body, name="fox_norm_bwd", grid=(NPAIR, NQ), in_specs=[blk, blk, vec],
        out_specs=[blk, pl.BlockSpec((None, BQ, 2), lambda hp, i: (hp, i, 0)), vec],
        out_shape=[jax.ShapeDtypeStruct((S, DFOX), BF16), jax.ShapeDtypeStruct((NPAIR, S, 2), F32),
                   jax.ShapeDtypeStruct((1, LANES), F32)],
        compiler_params=_cparams(),
    )(o, dmix, w2)


def _fox_bwd(proj, do, gates, gates_t, lse, delta):
    def body(q_ref, k_ref, v_ref, do_ref, g_ref, ct_ref, lse_ref, dl_ref, dproj_ref, dc_ref, qb_ref, dq_ref):
        hp, kj = pl.program_id(0), pl.program_id(1)

        @pl.when(kj == 0)
        def _():
            qb_ref[...] = q_ref[...].astype(BF16)
            dq_ref[...] = jnp.zeros_like(dq_ref)

        @pl.when((hp == 0) & (kj == 0))
        def _():
            dc_ref[...] = jnp.zeros_like(dc_ref)

        masks = [_head_mask(0), _head_mask(1)]
        kb = k_ref[...].astype(BF16)
        vb = v_ref[...].astype(BF16)
        ctk = ct_ref[...]
        cks = [_sublane_row(ctk, 2 * hp + e) for e in range(2)]
        lane = lax.broadcasted_iota(jnp.int32, (BQ, LANES), 1)

        def step(qi, carry):
            dk, dv, cs = carry
            rows = pl.ds(pl.multiple_of(qi * BQ, BQ), BQ)
            qv = qb_ref[rows, :]
            dov = do_ref[rows, :]
            gt = g_ref[rows, :]
            lse2 = lse_ref[rows, :]
            dl2 = dl_ref[rows, :]
            dq = jnp.zeros((BQ, LANES), F32)
            dc = jnp.zeros((BQ, LANES), F32)
            cs_new = []
            for e in range(2):
                h = 2 * hp + e
                qe = jnp.where(masks[e], qv, jnp.zeros_like(qv))
                doe = jnp.where(masks[e], dov, jnp.zeros_like(dov))
                s = _fox_logits(qe, kb, _lane_col(gt, h), cks[e], qi, kj)
                p = jnp.exp(s - _lane_col(lse2, e))
                dp = _dot(doe, vb, 1, 1)
                ds = p * (dp - _lane_col(dl2, e))
                dv = dv + _dot(p, doe, 0, 0)
                dk = dk + _dot(ds, qe, 0, 0) * FOX_SCALE
                dq = dq + jnp.where(masks[e], _dot(ds, kb), 0.0) * FOX_SCALE
                cs_new.append(cs[e] + jnp.sum(ds, axis=0, keepdims=True))
                dc = dc + jnp.where(lane == h, jnp.sum(ds, axis=1, keepdims=True), 0.0)
            dq_ref[rows, :] += dq
            dc_ref[rows, :] += dc
            return dk, dv, tuple(cs_new)

        zero = jnp.zeros((BQ, LANES), F32)
        dk, dv, cs = lax.fori_loop(kj, NQ, step, (zero, zero, (jnp.zeros((1, BQ), F32), jnp.zeros((1, BQ), F32))))
        krows = pl.ds(pl.multiple_of(kj * BQ, BQ), BQ)
        dproj_ref[krows, LANES:2 * LANES] = dk.astype(BF16)
        dproj_ref[krows, 2 * LANES:3 * LANES] = dv.astype(BF16)
        r = lax.broadcasted_iota(jnp.int32, (BQ, BQ), 0)
        c = lax.broadcasted_iota(jnp.int32, (BQ, BQ), 1)
        dcol = jnp.zeros((BQ, LANES), F32)
        for e in range(2):
            col = jnp.sum(jnp.where(r == c, cs[e], 0.0), axis=1, keepdims=True)
            dcol = dcol + jnp.where(lane == 2 * hp + e, col, 0.0)
        dc_ref[krows, :] -= dcol

        @pl.when(kj == NQ - 1)
        def _():
            dproj_ref[:, 0:LANES] = dq_ref[...].astype(BF16)

    def blk(part):
        return pl.BlockSpec((BQ, LANES), lambda hp, j: (j, 3 * hp + part))

    pair = pl.BlockSpec((None, S, 2), lambda hp, j: (hp, 0, 0))
    return pl.pallas_call(
        body, name="fox_bwd", grid=(NPAIR, NQ),
        in_specs=[pl.BlockSpec((S, LANES), lambda hp, j: (0, 3 * hp)), blk(1), blk(2),
                  pl.BlockSpec((S, LANES), lambda hp, j: (0, hp)), pl.BlockSpec((S, LANES), lambda hp, j: (0, 0)),
                  pl.BlockSpec((None, 8, BQ), lambda hp, j: (j, 0, 0)), pair, pair],
        out_specs=[pl.BlockSpec((S, 3 * LANES), lambda hp, j: (0, hp)), pl.BlockSpec((S, LANES), lambda hp, j: (0, 0))],
        out_shape=[jax.ShapeDtypeStruct((S, DPROJ_PAD), BF16), jax.ShapeDtypeStruct((S, LANES), F32)],
        scratch_shapes=[pltpu.VMEM((S, LANES), BF16), pltpu.VMEM((S, LANES), F32)],
        compiler_params=_cparams(),
    )(proj, proj, proj, do, gates, gates_t, lse, delta)


NQKV = 3 * NGH
GDN_QSCALE = GHD ** -0.5


def _shift_down(x, s):
    if s == 0:
        return x
    row = lax.broadcasted_iota(jnp.int32, x.shape, 0)
    return jnp.where(row >= s, pltpu.roll(x, s, 0), 0.0)


def _shift_up(x, s):
    if s == 0:
        return x
    n = x.shape[0]
    row = lax.broadcasted_iota(jnp.int32, x.shape, 0)
    return jnp.where(row < n - s, pltpu.roll(x, n - s, 0), 0.0)


def _conv_pre(xv, wv):
    pre = xv * wv[CONV_K - 1:CONV_K, :]
    for j in range(CONV_K - 1):
        pre = pre + _shift_down(xv, CONV_K - 1 - j) * wv[j:j + 1, :]
    return pre


def _l2_factors(b):
    return b < 2 * NGH, jnp.where(b < NGH, GDN_QSCALE, 1.0)


def _gdn_pre(proj, conv_w):
    def body(x_ref, w_ref, o_ref):
        b = pl.program_id(0)
        c = _silu(_conv_pre(x_ref[...], w_ref[...]))
        normed, scale = _l2_factors(b)
        rs = lax.rsqrt(jnp.sum(c * c, axis=-1, keepdims=True) + EPS)
        o_ref[...] = c * jnp.where(normed, rs, 1.0) * scale

    return pl.pallas_call(
        body, name="gdn_pre", grid=(NQKV,),
        in_specs=[pl.BlockSpec((S, GHD), lambda b: (0, BLK_GDN + b)), pl.BlockSpec((CONV_K, GHD), lambda b: (0, b))],
        out_specs=pl.BlockSpec((S, GHD), lambda b: (0, b)),
        out_shape=jax.ShapeDtypeStruct((S, NQKV * GHD), F32), compiler_params=_cparams(),
    )(proj, conv_w)


def _gdn_pre_bwd(proj, conv_w, dqkv, dproj):
    def body(x_ref, w_ref, dy_ref, dproj_in, dx_ref, dw_ref):
        del dproj_in
        b = pl.program_id(0)
        xv = x_ref[...]
        wv = w_ref[...]
        pre = _conv_pre(xv, wv)
        sig = _sigmoid(pre)
        c = pre * sig
        normed, scale = _l2_factors(b)
        g = dy_ref[...] * scale
        rs = lax.rsqrt(jnp.sum(c * c, axis=-1, keepdims=True) + EPS)
        dc_n = rs * g - c * (rs * rs * rs) * jnp.sum(g * c, axis=-1, keepdims=True)
        dc = jnp.where(normed, dc_n, g)
        dpre = dc * sig * (1.0 + pre * (1.0 - sig))
        dx = dpre * wv[CONV_K - 1:CONV_K, :]
        for j in range(CONV_K - 1):
            dx = dx + _shift_up(dpre, CONV_K - 1 - j) * wv[j:j + 1, :]
        dx_ref[...] = dx.astype(BF16)
        for j in range(CONV_K):
            dw_ref[j:j + 1, :] = jnp.sum(dpre * _shift_down(xv, CONV_K - 1 - j), axis=0, keepdims=True)

    return pl.pallas_call(
        body, name="gdn_pre_bwd", grid=(NQKV,),
        in_specs=[pl.BlockSpec((S, GHD), lambda b: (0, BLK_GDN + b)), pl.BlockSpec((CONV_K, GHD), lambda b: (0, b)),
                  pl.BlockSpec((None, S, GHD), lambda b: (b // NGH, 0, b % NGH)), pl.BlockSpec(memory_space=pl.ANY)],
        out_specs=[pl.BlockSpec((S, GHD), lambda b: (0, BLK_GDN + b)), pl.BlockSpec((CONV_K, GHD), lambda b: (0, b))],
        out_shape=[jax.ShapeDtypeStruct((S, DPROJ_PAD), BF16), jax.ShapeDtypeStruct((CONV_K, NQKV * GHD), F32)],
        input_output_aliases={3: 0}, compiler_params=_cparams(),
    )(proj, conv_w, dqkv, dproj)


CB = 4
NCB = NCH // CB


def _chunk_prep(q, k, v, gcol, bcol, t_saved=None):
    r = lax.broadcasted_iota(jnp.int32, (CHUNK, CHUNK), 0)
    c = lax.broadcasted_iota(jnp.int32, (CHUNK, CHUNK), 1)
    incl = c <= r
    eye = (r == c).astype(F32)
    grow = jnp.sum(gcol * eye, axis=0, keepdims=True)
    gc_col = jnp.sum(jnp.where(incl, grow, 0.0), axis=1, keepdims=True)
    gc_row = jnp.sum(jnp.where(r <= c, gcol, 0.0), axis=0, keepdims=True)
    decay = jnp.exp(jnp.where(incl, gc_col - gc_row, -jnp.inf))
    kb = k * bcol
    vb = v * bcol
    m = jnp.where(c < r, _mm_nt(kb, k) * decay, 0.0)
    if t_saved is None:
        t_inv = eye - m
        p = _hdot(m, m)
        for step in range(5):
            t_inv = t_inv + _hdot(t_inv, p)
            if step < 4:
                p = _hdot(p, p)
    else:
        t_inv = _saved_inverse(m, t_saved)
    egc = jnp.exp(gc_col)
    u = _mm_nn(t_inv, vb)
    w = _mm_nn(t_inv, kb * egc)
    a_intra = _mm_nt(q, k) * decay
    gc_last = gc_col[CHUNK - 1:CHUNK, :]
    return u, w, a_intra, q * egc, k * jnp.exp(gc_last - gc_col), jnp.exp(gc_last), t_inv


def _prep_specs():
    rows = CB * CHUNK
    qs = pl.BlockSpec((rows, GHD), lambda i, h: (i, h))
    ks = pl.BlockSpec((rows, GHD), lambda i, h: (i, NGH + h))
    vs = pl.BlockSpec((rows, GHD), lambda i, h: (i, 2 * NGH + h))
    gs = pl.BlockSpec((rows, LANES), lambda i, h: (i, 0))
    a_s = pl.BlockSpec((None, rows, CHUNK), lambda i, h: (h, i, 0))
    gl_s = pl.BlockSpec((None, CB, 1, LANES), lambda i, h: (h, i, 0, 0))
    return qs, ks, vs, gs, a_s, gl_s


def _gdn_prep(qkv, gates):
    def body(q_ref, k_ref, v_ref, g_ref, u_ref, w_ref, qd_ref, kd_ref, a_ref, gl_ref, t_ref):
        h = pl.program_id(1)
        for cidx in range(CB):
            rows = pl.ds(cidx * CHUNK, CHUNK)
            gt = g_ref[rows, :]
            u, w, a, qd, kd, gl, t_inv = _chunk_prep(q_ref[rows, :], k_ref[rows, :], v_ref[rows, :],
                                                     _lane_col(gt, LANE_G + h), _lane_col(gt, LANE_BETA + h))
            u_ref[rows, :] = u
            w_ref[rows, :] = w
            qd_ref[rows, :] = qd
            kd_ref[rows, :] = kd
            a_ref[rows, :] = a
            t_ref[rows, :] = t_inv
            gl_ref[cidx] = jnp.broadcast_to(gl, (1, LANES))

    qs, ks, vs, gs, a_s, gl_s = _prep_specs()
    tok = jax.ShapeDtypeStruct((S, DGDN), F32)
    sq = jax.ShapeDtypeStruct((NGH, S, CHUNK), F32)
    return pl.pallas_call(
        body, name="gdn_prep", grid=(NCB, NGH), in_specs=[qs, ks, vs, gs], out_specs=[qs, qs, qs, qs, a_s, gl_s, a_s],
        out_shape=[tok, tok, tok, tok, sq, jax.ShapeDtypeStruct((NGH, NCH, 1, LANES), F32), sq],
        compiler_params=_cparams(),
    )(qkv, qkv, qkv, gates)


def _gdn_prep_bwd(qkv, gates, t_inv, du, dw, dqd, dkd, da, dgl):
    def body(q_ref, k_ref, v_ref, g_ref, t_ref, du_ref, dw_ref, dqd_ref, dkd_ref, da_ref, dgl_ref, dqkv_ref, dg_ref):
        h = pl.program_id(1)

        @pl.when(h == 0)
        def _():
            dg_ref[...] = jnp.zeros_like(dg_ref)

        lane = lax.broadcasted_iota(jnp.int32, (CHUNK, LANES), 1)
        for cidx in range(CB):
            rows = pl.ds(cidx * CHUNK, CHUNK)
            gt = g_ref[rows, :]
            t_saved = t_ref[rows, :]
            _, vjp = jax.vjp(lambda *args: _chunk_prep(*args, t_saved=t_saved)[:6], q_ref[rows, :], k_ref[rows, :],
                             v_ref[rows, :], _lane_col(gt, LANE_G + h), _lane_col(gt, LANE_BETA + h))
            dq, dk, dv, dgc, dbc = vjp((du_ref[rows, :], dw_ref[rows, :], da_ref[rows, :], dqd_ref[rows, :],
                                        dkd_ref[rows, :], dgl_ref[cidx][:, 0:1]))
            dqkv_ref[0, rows, :] = dq
            dqkv_ref[1, rows, :] = dk
            dqkv_ref[2, rows, :] = dv
            dg_ref[rows, :] += jnp.where(lane == LANE_G + h, dgc, 0.0) + jnp.where(lane == LANE_BETA + h, dbc, 0.0)

    qs, ks, vs, gs, a_s, gl_s = _prep_specs()
    return pl.pallas_call(
        body, name="gdn_prep_bwd", grid=(NCB, NGH), in_specs=[qs, ks, vs, gs, a_s, qs, qs, qs, qs, a_s, gl_s],
        out_specs=[pl.BlockSpec((3, CB * CHUNK, GHD), lambda i, h: (0, i, h)), gs],
        out_shape=[jax.ShapeDtypeStruct((3, S, DGDN), F32), jax.ShapeDtypeStruct((S, LANES), F32)],
        compiler_params=_cparams(),
    )(qkv, qkv, qkv, gates, t_inv, du, dw, dqd, dkd, da, dgl)


def _scan_specs():
    hs = pl.BlockSpec((S, GHD), lambda h: (0, h))
    a_s = pl.BlockSpec((None, S, CHUNK), lambda h: (h, 0, 0))
    gl_s = pl.BlockSpec((None, NCH, 1, LANES), lambda h: (h, 0, 0, 0))
    st_s = pl.BlockSpec((None, NCH, GHD, GHD), lambda h: (h, 0, 0, 0))
    gz_s = pl.BlockSpec((S, GHD), lambda h: (0, BLK_GZ + h))
    mix_s = pl.BlockSpec((S, GHD), lambda h: (0, NPAIR + h))
    return hs, a_s, gl_s, st_s, gz_s, mix_s


def _gdn_scan(u, w, qd, kd, a, gl, proj, w_norm, mix):
    def body(u_ref, w_ref, qd_ref, kd_ref, a_ref, gl_ref, z_ref, wn_ref, mix_in, mix_ref, o_ref, st_ref):
        del mix_in

        def step(ci, state):
            rows = pl.ds(pl.multiple_of(ci * CHUNK, CHUNK), CHUNK)
            st_ref[ci] = state
            vn = u_ref[rows, :] - _dot(w_ref[rows, :], state)
            o_ref[rows, :] = _dot(qd_ref[rows, :], state) + _dot(a_ref[rows, :], vn)
            return state * gl_ref[ci] + _dot(kd_ref[rows, :], vn, 0, 0)

        lax.fori_loop(0, NCH, step, jnp.zeros((GHD, GHD), F32))
        ov = o_ref[...]
        mix_ref[...] = (ov * _rms_scale(ov) * wn_ref[...] * _silu(z_ref[...])).astype(BF16)

    hs, a_s, gl_s, st_s, gz_s, mix_s = _scan_specs()
    return pl.pallas_call(
        body, name="gdn_scan", grid=(NGH,),
        in_specs=[hs, hs, hs, hs, a_s, gl_s, gz_s, pl.BlockSpec((1, GHD), lambda h: (0, 0)), pl.BlockSpec(memory_space=pl.ANY)],
        out_specs=[mix_s, hs, st_s],
        out_shape=[jax.ShapeDtypeStruct((S, D), BF16), jax.ShapeDtypeStruct((S, DGDN), F32),
                   jax.ShapeDtypeStruct((NGH, NCH, GHD, GHD), F32)],
        input_output_aliases={8: 0}, compiler_params=_cparams(),
    )(u, w, qd, kd, a, gl, proj, w_norm, mix)


def _gdn_scan_bwd(dmix, o, proj, w_norm, u, w, qd, kd, a, gl, states, dproj):
    def body(dy_ref, o_ref, z_ref, wn_ref, u_ref, w_ref, qd_ref, kd_ref, a_ref, gl_ref, st_ref, dproj_in,
             dz_ref, du_ref, dw_ref, dqd_ref, dkd_ref, da_ref, dgl_ref, dwn_ref, do_ref):
        del dproj_in
        h = pl.program_id(0)
        ov = o_ref[...]
        zv = z_ref[...]
        wn = wn_ref[...]
        g = dy_ref[...]
        sig = _sigmoid(zv)
        dz_ref[...] = (g * (ov * _rms_scale(ov) * wn) * sig * (1.0 + zv * (1.0 - sig))).astype(BF16)
        do, dwt = _rms_bwd(ov, wn, g * zv * sig)
        do_ref[...] = do

        @pl.when(h == 0)
        def _():
            dwn_ref[...] = jnp.zeros_like(dwn_ref)

        dwn_ref[...] += jnp.sum(dwt, axis=0, keepdims=True)

        def step(t, dstate):
            ci = NCH - 1 - t
            rows = pl.ds(pl.multiple_of(ci * CHUNK, CHUNK), CHUNK)
            state = st_ref[ci]
            dov = do_ref[rows, :]
            wv = w_ref[rows, :]
            kdv = kd_ref[rows, :]
            vn = u_ref[rows, :] - _dot(wv, state)
            dvn = _dot(a_ref[rows, :], dov, 0, 0) + _dot(kdv, dstate)
            da_ref[rows, :] = _dot(dov, vn, 1, 1)
            dqd_ref[rows, :] = _dot(dov, state, 1, 1)
            dkd_ref[rows, :] = _dot(vn, dstate, 1, 1)
            dgl = jnp.sum(jnp.sum(dstate * state, axis=1, keepdims=True), axis=0, keepdims=True)
            dgl_ref[ci] = jnp.broadcast_to(dgl, (1, LANES))
            du_ref[rows, :] = dvn
            dw_ref[rows, :] = -_dot(dvn, state, 1, 1)
            return dstate * gl_ref[ci] + _dot(qd_ref[rows, :], dov, 0, 0) - _dot(wv, dvn, 0, 0)

        lax.fori_loop(0, NCH, step, jnp.zeros((GHD, GHD), F32))

    hs, a_s, gl_s, st_s, gz_s, mix_s = _scan_specs()
    vec = pl.BlockSpec((1, GHD), lambda h: (0, 0))
    tok = jax.ShapeDtypeStruct((S, DGDN), F32)
    return pl.pallas_call(
        body, name="gdn_scan_bwd", grid=(NGH,),
        in_specs=[mix_s, hs, gz_s, vec, hs, hs, hs, hs, a_s, gl_s, st_s, pl.BlockSpec(memory_space=pl.ANY)],
        out_specs=[gz_s, hs, hs, hs, hs, a_s, gl_s, vec],
        out_shape=[jax.ShapeDtypeStruct((S, DPROJ_PAD), BF16), tok, tok, tok, tok,
                   jax.ShapeDtypeStruct((NGH, S, CHUNK), F32), jax.ShapeDtypeStruct((NGH, NCH, 1, LANES), F32),
                   jax.ShapeDtypeStruct((1, GHD), F32)],
        input_output_aliases={11: 0}, scratch_shapes=[pltpu.VMEM((S, GHD), F32)], compiler_params=_cparams(),
    )(dmix, o, proj, w_norm, u, w, qd, kd, a, gl, states, dproj)


def _place():
    return lax.axis_index("x"), lax.axis_index("y"), lax.axis_index("c")


def _other_chips(x, y):
    return [(1 - x, y), (x, 1 - y), (1 - x, 1 - y)]


HBM = pl.BlockSpec(memory_space=pltpu.HBM)
VMEM = pl.BlockSpec(memory_space=pltpu.VMEM)


def _half_rows(ref_or_rows, half):
    rows = ref_or_rows // 2
    return pl.ds(pl.multiple_of(half * rows, rows), rows)


def _weights_allgather(shards, conv):
    n = len(shards)

    def body(*refs):
        src, conv_src = refs[:n], refs[n]
        outs, conv_out = refs[n + 1:2 * n + 1], refs[2 * n + 1]
        send_sems, recv_sems, local_sems = refs[2 * n + 2:]
        x, y, c = _place()
        me_chip = 2 * x + y
        sibling = (x, y, 1 - c)
        chips = _other_chips(x, y)
        chip_ids = [2 * ch[0] + ch[1] for ch in chips]

        def copy(a, k, chip_index, half, to, from_src):
            rows = _half_rows(src[a].shape[0], half)
            dst = outs[a].at[chip_index, rows]
            return pltpu.make_async_remote_copy(
                src_ref=src[a].at[rows] if from_src else dst, dst_ref=dst, send_sem=send_sems.at[6 * a + k],
                recv_sem=recv_sems.at[6 * a + k], device_id=to, device_id_type=MESH)

        def conv_copy(k, chip_index, to):
            return pltpu.make_async_remote_copy(
                src_ref=conv_src, dst_ref=conv_out.at[chip_index], send_sem=send_sems.at[6 * n + k],
                recv_sem=recv_sems.at[6 * n + k], device_id=to, device_id_type=MESH)

        local = [pltpu.make_async_copy(src[a], outs[a].at[me_chip], local_sems.at[a]) for a in range(n)]
        local.append(pltpu.make_async_copy(conv_src, conv_out.at[me_chip], local_sems.at[n]))
        for cp in local:
            cp.start()
        first = [copy(a, j, me_chip, c, (*chips[j], c), True) for a in range(n) for j in range(3)]
        first += [conv_copy(j, me_chip, (*chips[j], c)) for j in range(3)]
        for cp in first:
            cp.start()
        passed = []
        for a in range(n):
            for j in range(3):
                copy(a, j, chip_ids[j], c, (x, y, c), False).wait_recv()
                cp = copy(a, 3 + j, chip_ids[j], c, sibling, False)
                cp.start()
                passed.append(cp)
        for a in range(n):
            for j in range(3):
                copy(a, 3 + j, chip_ids[j], 1 - c, (x, y, c), False).wait_recv()
        for j in range(3):
            conv_copy(j, chip_ids[j], (x, y, c)).wait_recv()
        for cp in first + passed:
            cp.wait_send()
        for cp in local:
            cp.wait()

    n_sem = 6 * n + 3
    return pl.pallas_call(
        body, name="weights_allgather", in_specs=[HBM] * (n + 1), out_specs=[HBM] * (n + 1),
        out_shape=[jax.ShapeDtypeStruct((NCHIP,) + s.shape, s.dtype) for s in shards]
        + [jax.ShapeDtypeStruct((NCHIP,) + conv.shape, conv.dtype)],
        scratch_shapes=[pltpu.SemaphoreType.DMA((n_sem,)), pltpu.SemaphoreType.DMA((n_sem,)),
                        pltpu.SemaphoreType.DMA((n + 1,))],
        compiler_params=_cparams(),
    )(*shards, conv)


def _grads_pair_exchange(grads):
    n = len(grads)

    def body(*refs):
        src = refs[:n]
        mine, theirs = refs[n:2 * n], refs[2 * n:3 * n]
        send_sems, recv_sems, local_sems = refs[3 * n:]
        x, y, c = _place()
        copies = []
        for a in range(n):
            rows = src[a].shape[1]
            cp = pltpu.make_async_copy(src[a].at[:, _half_rows(rows, c)], mine[a], local_sems.at[a])
            cp.start()
            copies.append(cp)
        for a in range(n):
            rows = src[a].shape[1]
            cp = pltpu.make_async_remote_copy(
                src_ref=src[a].at[:, _half_rows(rows, 1 - c)], dst_ref=theirs[a], send_sem=send_sems.at[a],
                recv_sem=recv_sems.at[a], device_id=(x, y, 1 - c), device_id_type=MESH)
            cp.start()
            copies.append(cp)
        for cp in copies:
            cp.wait()

    halves = [jax.ShapeDtypeStruct((g.shape[0], g.shape[1] // 2, g.shape[2]), g.dtype) for g in grads]
    res = pl.pallas_call(
        body, name="grads_pair_exchange", in_specs=[HBM] * n, out_specs=[HBM] * (2 * n), out_shape=halves + halves,
        scratch_shapes=[pltpu.SemaphoreType.DMA((n,)), pltpu.SemaphoreType.DMA((n,)), pltpu.SemaphoreType.DMA((n,))],
        compiler_params=_cparams(),
    )(*grads)
    return res[:n], res[n:]


def _pair_sum(mine, theirs):
    n = len(mine)

    def body(*refs):
        for a in range(n):
            refs[2 * n + a][...] = (refs[a][...].astype(F32) + refs[n + a][...].astype(F32)).astype(BF16)

    specs = [pl.BlockSpec((None,) + g.shape[1:], lambda j: (j, 0, 0)) for g in mine]
    return pl.pallas_call(
        body, name="grads_pair_sum", grid=(NCHIP,), in_specs=specs + specs, out_specs=specs,
        out_shape=[jax.ShapeDtypeStruct(g.shape, BF16) for g in mine], compiler_params=_cparams(),
    )(*mine, *theirs)


def _grads_chip_exchange(parts):
    n = len(parts)

    def body(*refs):
        src, outs = refs[:n], refs[n:2 * n]
        send_sems, recv_sems, local_sems = refs[2 * n:]
        x, y, c = _place()
        copies = []
        for a in range(n):
            cp = pltpu.make_async_copy(src[a].at[2 * x + y], outs[a].at[3], local_sems.at[a])
            cp.start()
            copies.append(cp)
        for a in range(n):
            for k, chip in enumerate(_other_chips(x, y)):
                cp = pltpu.make_async_remote_copy(
                    src_ref=src[a].at[2 * chip[0] + chip[1]], dst_ref=outs[a].at[k], send_sem=send_sems.at[3 * a + k],
                    recv_sem=recv_sems.at[3 * a + k], device_id=(*chip, c), device_id_type=MESH)
                cp.start()
                copies.append(cp)
        for cp in copies:
            cp.wait()

    return pl.pallas_call(
        body, name="grads_chip_exchange", in_specs=[HBM] * n, out_specs=[HBM] * n,
        out_shape=[jax.ShapeDtypeStruct(p.shape, p.dtype) for p in parts],
        scratch_shapes=[pltpu.SemaphoreType.DMA((3 * n,)), pltpu.SemaphoreType.DMA((3 * n,)), pltpu.SemaphoreType.DMA((n,))],
        compiler_params=_cparams(),
    )(*parts)


def _chip_sum(slots):
    n = len(slots)
    steps = 4

    def body(*refs):
        for a in range(n):
            src = refs[a]
            refs[n + a][...] = ((src[3].astype(F32) + src[0].astype(F32)) + src[1].astype(F32)) + src[2].astype(F32)

    in_specs = [pl.BlockSpec((NCHIP, g.shape[1] // steps, g.shape[2]), lambda i: (0, i, 0)) for g in slots]
    out_specs = [pl.BlockSpec((g.shape[1] // steps, g.shape[2]), lambda i: (i, 0)) for g in slots]
    return pl.pallas_call(
        body, name="grads_chip_sum", grid=(steps,), in_specs=in_specs, out_specs=out_specs,
        out_shape=[jax.ShapeDtypeStruct(g.shape[1:], F32) for g in slots], compiler_params=_cparams(),
    )(*slots)


def _grads_pair_share(halves):
    n = len(halves)

    def body(*refs):
        src, outs = refs[:n], refs[n:2 * n]
        send_sems, recv_sems, local_sems = refs[2 * n:]
        x, y, c = _place()
        copies = []
        for a in range(n):
            rows = _half_rows(2 * src[a].shape[0], c)
            cp = pltpu.make_async_copy(src[a], outs[a].at[rows], local_sems.at[a])
            cp.start()
            copies.append(cp)
            cp = pltpu.make_async_remote_copy(
                src_ref=src[a], dst_ref=outs[a].at[rows], send_sem=send_sems.at[a], recv_sem=recv_sems.at[a],
                device_id=(x, y, 1 - c), device_id_type=MESH)
            cp.start()
            copies.append(cp)
        for cp in copies:
            cp.wait()

    return pl.pallas_call(
        body, name="grads_pair_share", in_specs=[HBM] * n, out_specs=[HBM] * n,
        out_shape=[jax.ShapeDtypeStruct((2 * h.shape[0], h.shape[1]), F32) for h in halves],
        scratch_shapes=[pltpu.SemaphoreType.DMA((n,)), pltpu.SemaphoreType.DMA((n,)), pltpu.SemaphoreType.DMA((n,))],
        compiler_params=_cparams(),
    )(*halves)


def _adamw_math(w, g, m, v):
    nm = ADAM_B1 * m + (1.0 - ADAM_B1) * g
    nv = ADAM_B2 * v + (1.0 - ADAM_B2) * jnp.square(g)
    m_hat = nm / (1.0 - ADAM_B1 ** ADAM_STEP)
    v_hat = nv / (1.0 - ADAM_B2 ** ADAM_STEP)
    return -ADAM_LR * (m_hat / (jnp.sqrt(v_hat) + ADAM_EPS) + ADAM_WD * w), nm, nv


def _adamw_big(ws, gs, ms, vs):
    n = len(ws)
    steps = 8

    def body(*refs):
        for a in range(n):
            d, nm, nv = _adamw_math(refs[a][...], refs[n + a][...], refs[2 * n + a][...], refs[3 * n + a][...])
            refs[4 * n + a][...] = d
            refs[5 * n + a][...] = nm
            refs[6 * n + a][...] = nv

    specs = [pl.BlockSpec((w.shape[0] // steps, w.shape[1]), lambda i: (i, 0)) for w in ws]
    shapes = [jax.ShapeDtypeStruct(w.shape, F32) for w in ws]
    res = pl.pallas_call(
        body, name="adamw_big", grid=(steps,), in_specs=specs * 4, out_specs=specs * 3, out_shape=shapes * 3,
        compiler_params=_cparams(),
    )(*ws, *gs, *ms, *vs)
    return res[:n], res[n:2 * n], res[2 * n:]


NORM_NAMES = ("pre_mix_norm", "post_mix_norm", "pre_mlp_norm", "post_mlp_norm")
SMALL_NAMES = NORM_NAMES + ("gdn_conv_w", "fox_f_bias", "gdn_dt_bias", "gdn_a_log", "fox_out_norm", "gdn_out_norm")
CONV_COLS = 3 * DGDN // NCHIP


def _small_update(d_norms, d_conv, sums, d_fox_norm, d_gdn_norm, ws, ms, vs):
    n = len(SMALL_NAMES)
    shapes = [w.shape for w in ws]

    def body(*refs):
        dn_ref, dconv_ref, sums_ref, dfn_ref, dgn_ref = refs[:5]
        w_refs, m_refs, v_refs = refs[5:5 + n], refs[5 + n:5 + 2 * n], refs[5 + 2 * n:5 + 3 * n]
        outs = refs[5 + 3 * n:5 + 7 * n]
        g_norms, g_conv, g_sums, g_fn, g_gn, send_sems, recv_sems, local_sem = refs[5 + 7 * n:]
        x, y, c = _place()
        me = 4 * x + 2 * y + c
        g_norms[me] = dn_ref[...]
        g_sums[me] = sums_ref[...]
        g_fn[me] = dfn_ref[...]
        g_gn[me] = dgn_ref[...]

        def conv_cols(chip_index):
            return dconv_ref.at[:, pl.ds(pl.multiple_of(chip_index * CONV_COLS, LANES), CONV_COLS)]

        own = pltpu.make_async_copy(conv_cols(2 * x + y), g_conv.at[me], local_sem)
        own.start()
        copies = []
        for k in range(1, NDEV):
            px, py, pc = x ^ ((k >> 2) & 1), y ^ ((k >> 1) & 1), c ^ (k & 1)
            pairs = [(dn_ref, g_norms), (conv_cols(2 * px + py), g_conv), (sums_ref, g_sums), (dfn_ref, g_fn), (dgn_ref, g_gn)]
            for a, (src, dst) in enumerate(pairs):
                cp = pltpu.make_async_remote_copy(
                    src_ref=src, dst_ref=dst.at[me], send_sem=send_sems.at[5 * (k - 1) + a],
                    recv_sem=recv_sems.at[5 * (k - 1) + a], device_id=(px, py, pc), device_id_type=MESH)
                cp.start()
                copies.append(cp)
        own.wait()
        for cp in copies:
            cp.wait()

        def total(buf):
            acc = buf[0]
            for i in range(1, NDEV):
                acc = acc + buf[i]
            return acc

        t_norms, t_conv, t_sums, t_fn, t_gn = total(g_norms), total(g_conv), total(g_sums), total(g_fn), total(g_gn)
        grads = [t_norms[i:i + 1, :] for i in range(4)] + [
            t_conv, t_sums[0:1, 0:NFH], t_sums[1:2, 0:NGH], t_sums[2:3, 0:NGH], t_fn[:, 0:FHD], t_gn]
        for a in range(n):
            d, nm, nv = _adamw_math(w_refs[a][...], grads[a], m_refs[a][...], v_refs[a][...])
            outs[a][...] = grads[a]
            outs[n + a][...] = d
            outs[2 * n + a][...] = nm
            outs[3 * n + a][...] = nv

    n_sem = 5 * (NDEV - 1)
    out_shape = [jax.ShapeDtypeStruct(s, F32) for s in shapes] * 4
    res = pl.pallas_call(
        body, name="small_update", in_specs=[VMEM] * (5 + 3 * n), out_specs=[VMEM] * (4 * n), out_shape=out_shape,
        scratch_shapes=[pltpu.VMEM((NDEV, 4, D), F32), pltpu.VMEM((NDEV, CONV_K, CONV_COLS), F32),
                        pltpu.VMEM((NDEV, 8, LANES), F32), pltpu.VMEM((NDEV, 1, LANES), F32),
                        pltpu.VMEM((NDEV, 1, LANES), F32), pltpu.SemaphoreType.DMA((n_sem,)),
                        pltpu.SemaphoreType.DMA((n_sem,)), pltpu.SemaphoreType.DMA],
        compiler_params=_cparams(),
    )(d_norms, d_conv, sums, d_fox_norm, d_gdn_norm, *ws, *ms, *vs)
    return res[:n], res[n:2 * n], res[2 * n:3 * n], res[3 * n:]


def _to_padded_cols(w):
    pieces = [w[:, part * DFOX + hp * LANES:part * DFOX + (hp + 1) * LANES] for hp in range(NPAIR) for part in range(3)]
    pieces += [w[:, 1544:3080], w[:, 3088:3600], w[:, 1536:1544], w[:, 3080:3088],
               jnp.zeros((w.shape[0], 2 * LANES - 16), w.dtype)]
    return jnp.concatenate(pieces, axis=1)


def _from_padded_cols(w):
    c0 = BLK_SMALL * LANES
    fox = [w[:, (3 * hp + part) * LANES:(3 * hp + part + 1) * LANES] for part in range(3) for hp in range(NPAIR)]
    return jnp.concatenate(fox + [w[:, c0:c0 + 8], w[:, BLK_GDN * LANES:BLK_GZ * LANES], w[:, c0 + 8:c0 + 16],
                                  w[:, BLK_GZ * LANES:BLK_SMALL * LANES]], axis=1)


def _local_step(x, target, win_p, wout, wup3, wdown, pre_mix_norm, fox_f_bias, fox_out_norm, conv_w, gdn_a_log,
                gdn_dt_bias, gdn_out_norm, post_mix_norm, pre_mlp_norm, post_mlp_norm):
    bias_vec = jnp.zeros((1, LANES), F32).at[0, 0:NFH].set(fox_f_bias).at[0, LANE_G:LANE_G + NGH].set(gdn_dt_bias)
    alog_vec = jnp.zeros((1, LANES), F32).at[0, LANE_G:LANE_G + NGH].set(gdn_a_log)
    w2 = jnp.concatenate([fox_out_norm, fox_out_norm], axis=1)

    h = _pre_norm(x, pre_mix_norm)
    proj = _matmul(h, win_p, tn=768, tk=1024, name="mm_proj")
    gates, gates_t = _gates(proj, bias_vec, alog_vec)
    mix, fox_o, lse = _fox_fwd(proj, gates, gates_t, w2)
    qkv = _gdn_pre(proj, conv_w)
    u, w, qd, kd, a_intra, gl, t_inv = _gdn_prep(qkv, gates)
    mix, gdn_raw, states = _gdn_scan(u, w, qd, kd, a_intra, gl, proj, gdn_out_norm, mix)
    mixed = _matmul(mix, wout, tk=1024, name="mm_out")
    x1, h2 = _post_mix(x, mixed, post_mix_norm, pre_mlp_norm)

    def relu2(acc):
        r = jnp.maximum(acc, 0.0)
        return acc, r * r

    up, act = _matmul(h2, wup3, b3=True, tn=1024, tk=1024, out_dtypes=(F32, BF16), epilogue=relu2, name="mm_up")
    y = _matmul(act, wdown, tk=1024, name="mm_down")
    dx2, dy, d_post_mlp, loss_row = _loss_head(x1, y, post_mlp_norm, target)

    dwdown = _matmul(act, dy, ta=True, tk=1024, out_dtypes=(BF16,), name="mm_dwdown")

    def relu2_bwd(acc, upv):
        return (acc * 2.0 * jnp.maximum(upv, 0.0),)

    dup = _matmul(dy, wdown, tb=True, tk=1024, out_dtypes=(BF16,), extra=(up,), epilogue=relu2_bwd, name="mm_dact")
    dwup3 = _matmul(h2, dup, ta=True, tn=1024, tk=1024, out_dtypes=(BF16,), o3=True, name="mm_dwup")
    dh2 = _matmul(dup, wup3, tb=True, b3=True, tk=1024, name="mm_dh2")
    dx1, dmixed, d_pre_mlp, d_post_mix = _mid_bwd(dh2, x1, pre_mlp_norm, dx2, mixed, post_mix_norm)
    dwout = _matmul(mix, dmixed, ta=True, tk=1024, out_dtypes=(BF16,), name="mm_dwout")
    dmix = _matmul(dmixed, wout, tb=True, tk=1024, name="mm_dmix")

    dfox, delta, d_fox_norm = _fox_norm_bwd(fox_o, dmix, w2)
    dproj, dcum_fox = _fox_bwd(proj, dfox, gates, gates_t, lse, delta)
    dproj, du, dw, dqd, dkd, da, dgl, d_gdn_norm = _gdn_scan_bwd(dmix, gdn_raw, proj, gdn_out_norm, u, w, qd, kd,
                                                                 a_intra, gl, states, dproj)
    dqkv, dgates_gdn = _gdn_prep_bwd(qkv, gates, t_inv, du, dw, dqd, dkd, da, dgl)
    dproj, d_conv = _gdn_pre_bwd(proj, conv_w, dqkv, dproj)
    dproj, sums = _gates_bwd(proj, bias_vec, alog_vec, dgates_gdn, dcum_fox, dproj)

    dwin_p = _matmul(h, dproj, ta=True, tn=768, tk=1024, out_dtypes=(BF16,), name="mm_dwin")
    dh = _matmul(dproj, win_p, tb=True, tk=768, name="mm_dh")
    grad_x, d_pre_mix = _pre_norm_bwd(dh, x, pre_mix_norm, dx1)

    d_norms = jnp.concatenate([d_pre_mix, d_post_mix, d_pre_mlp, d_post_mlp], axis=0)
    return loss_row[0, 0], grad_x, (dwin_p, dwout, dwup3, dwdown), (d_norms, d_conv, sums, d_fox_norm, d_gdn_norm)


def kernel(x, pre_mix_norm, w_in, fox_f_bias, fox_out_norm, gdn_conv_w, gdn_a_log, gdn_dt_bias, gdn_out_norm, w_out, post_mix_norm, pre_mlp_norm, w_up, w_down, post_mlp_norm, loss_target, m_pre_mix_norm, m_w_in, m_fox_f_bias, m_fox_out_norm, m_gdn_conv_w, m_gdn_a_log, m_gdn_dt_bias, m_gdn_out_norm, m_w_out, m_post_mix_norm, m_pre_mlp_norm, m_w_up, m_w_down, m_post_mlp_norm, v_pre_mix_norm, v_w_in, v_fox_f_bias, v_fox_out_norm, v_gdn_conv_w, v_gdn_a_log, v_gdn_dt_bias, v_gdn_out_norm, v_w_out, v_post_mix_norm, v_pre_mlp_norm, v_w_up, v_w_down, v_post_mlp_norm):
    weights = dict(pre_mix_norm=pre_mix_norm, w_in=w_in, fox_f_bias=fox_f_bias, fox_out_norm=fox_out_norm, gdn_conv_w=gdn_conv_w,
                   gdn_a_log=gdn_a_log, gdn_dt_bias=gdn_dt_bias, gdn_out_norm=gdn_out_norm, w_out=w_out, post_mix_norm=post_mix_norm,
                   pre_mlp_norm=pre_mlp_norm, w_up=w_up, w_down=w_down, post_mlp_norm=post_mlp_norm)
    m_in = dict(pre_mix_norm=m_pre_mix_norm, w_in=m_w_in, fox_f_bias=m_fox_f_bias, fox_out_norm=m_fox_out_norm, gdn_conv_w=m_gdn_conv_w,
                gdn_a_log=m_gdn_a_log, gdn_dt_bias=m_gdn_dt_bias, gdn_out_norm=m_gdn_out_norm, w_out=m_w_out, post_mix_norm=m_post_mix_norm,
                pre_mlp_norm=m_pre_mlp_norm, w_up=m_w_up, w_down=m_w_down, post_mlp_norm=m_post_mlp_norm)
    v_in = dict(pre_mix_norm=v_pre_mix_norm, w_in=v_w_in, fox_f_bias=v_fox_f_bias, fox_out_norm=v_fox_out_norm, gdn_conv_w=v_gdn_conv_w,
                gdn_a_log=v_gdn_a_log, gdn_dt_bias=v_gdn_dt_bias, gdn_out_norm=v_gdn_out_norm, w_out=v_w_out, post_mix_norm=v_post_mix_norm,
                pre_mlp_norm=v_pre_mlp_norm, w_up=v_w_up, w_down=v_w_down, post_mlp_norm=v_post_mlp_norm)
    order_w = ("pre_mix_norm", "w_in", "fox_f_bias", "fox_out_norm", "gdn_conv_w", "gdn_a_log", "gdn_dt_bias", "gdn_out_norm", "w_out",
               "post_mix_norm", "pre_mlp_norm", "w_up", "w_down", "post_mlp_norm")
    big = ("w_in", "w_out", "w_up", "w_down")

    def row(v):
        return v if v.ndim == 2 else v.reshape(1, -1)

    gathered = _weights_allgather([weights[n].astype(BF16) for n in big], gdn_conv_w)
    win_g, wout_g, wup3, wdown_g, conv_g = gathered
    win_p = _to_padded_cols(win_g.transpose(1, 0, 2).reshape(D, DPROJ))
    conv_full = conv_g.transpose(1, 0, 2).reshape(CONV_K, 3 * DGDN)

    loss_local, grad_x, (dwin_p, dwout, dwup3, dwdown), small = _local_step(
        x[0], loss_target[0], win_p, wout_g.reshape(D, D), wup3, wdown_g.reshape(DFF, D), row(pre_mix_norm), fox_f_bias,
        row(fox_out_norm), conv_full, gdn_a_log, gdn_dt_bias, row(gdn_out_norm), row(post_mix_norm), row(pre_mlp_norm),
        row(post_mlp_norm))
    loss = lax.psum(loss_local, ("x", "y", "c"))

    dwin3 = _from_padded_cols(dwin_p).reshape(D, NCHIP, DPROJ // NCHIP).transpose(1, 0, 2)
    blocks = [dwin3, dwout.reshape(NCHIP, D // NCHIP, D), dwup3, dwdown.reshape(NCHIP, DFF // NCHIP, D)]
    mine, theirs = _grads_pair_exchange(blocks)
    slots = _grads_chip_exchange(_pair_sum(mine, theirs))
    g_big = _grads_pair_share(_chip_sum(slots))

    g_small, d_small, nm_small, nv_small = _small_update(
        *small, [row(weights[n]) for n in SMALL_NAMES], [row(m_in[n]) for n in SMALL_NAMES], [row(v_in[n]) for n in SMALL_NAMES])

    d_big, nm_big, nv_big = _adamw_big([weights[n] for n in big], g_big, [m_in[n] for n in big], [v_in[n] for n in big])

    grads, delta, new_m, new_v = {}, {}, {}, {}
    for i, n in enumerate(big):
        grads[n], delta[n], new_m[n], new_v[n] = g_big[i], d_big[i], nm_big[i], nv_big[i]
    for i, n in enumerate(SMALL_NAMES):
        shape = weights[n].shape
        grads[n], delta[n], new_m[n], new_v[n] = (g_small[i].reshape(shape), d_small[i].reshape(shape),
                                                  nm_small[i].reshape(shape), nv_small[i].reshape(shape))
    return (loss, grad_x[None], *[grads[n] for n in order_w], *[delta[n] for n in order_w], *[new_m[n] for n in order_w],
            *[new_v[n] for n in order_w])
```

```python
import jax
import jax.numpy as jnp
from jax import lax
from jax.experimental import pallas as pl
from jax.experimental.pallas import tpu as pltpu

F32 = jnp.float32
BF16 = jnp.bfloat16
MESH = pl.DeviceIdType.MESH

S = 2048
D = 1024
NFH, FHD = 8, 64
NPAIR = NFH // 2
NGH, GHD = 4, 128
DFOX = NFH * FHD
DGDN = NGH * GHD
CHUNK = 64
NCH = S // CHUNK
CONV_K = 4
DFF = 4 * D
EPS = 1e-6
DPROJ = 3600
LANES = 128
DPROJ_PAD = 3840
BLK_GDN = 12
BLK_GZ = 24
BLK_SMALL = 28
NCHIP = 4
NDEV = 8
VMEM_LIMIT = 56 * 1024 * 1024

ADAM_LR = 0.001
ADAM_B1 = 0.9
ADAM_B2 = 0.999
ADAM_EPS = 1e-08
ADAM_WD = 0.01
ADAM_STEP = 10


def _cparams(**kw):
    return pltpu.CompilerParams(vmem_limit_bytes=VMEM_LIMIT, **kw)


def _dn(ca, cb):
    return (((ca,), (cb,)), ((), ()))


def _dot(a, b, ca=1, cb=0):
    return lax.dot_general(a.astype(BF16), b.astype(BF16), _dn(ca, cb), preferred_element_type=F32)


def _hdot(a, b, ca=1, cb=0):
    return lax.dot_general(a.astype(F32), b.astype(F32), _dn(ca, cb), precision=lax.Precision.HIGHEST,
                           preferred_element_type=F32)


@jax.custom_vjp
def _mm_nn(a, b):
    return _dot(a, b, 1, 0)


def _mm_nn_fwd(a, b):
    return _dot(a, b, 1, 0), (a, b)


def _mm_nn_bwd(res, g):
    a, b = res
    return _dot(g, b, 1, 1), _dot(a, g, 0, 0)


_mm_nn.defvjp(_mm_nn_fwd, _mm_nn_bwd)


@jax.custom_vjp
def _mm_nt(a, b):
    return _dot(a, b, 1, 1)


def _mm_nt_fwd(a, b):
    return _dot(a, b, 1, 1), (a, b)


def _mm_nt_bwd(res, g):
    a, b = res
    return _dot(g, b, 1, 0), _dot(g, a, 0, 0)


_mm_nt.defvjp(_mm_nt_fwd, _mm_nt_bwd)


@jax.custom_vjp
def _saved_inverse(m, t_inv):
    del m
    return t_inv


def _saved_inverse_fwd(m, t_inv):
    del m
    return t_inv, t_inv


def _saved_inverse_bwd(t_inv, g):
    return -_hdot(_hdot(t_inv, g, 0, 0), t_inv, 1, 1), jnp.zeros_like(t_inv)


_saved_inverse.defvjp(_saved_inverse_fwd, _saved_inverse_bwd)


def _sigmoid(z):
    return 1.0 / (1.0 + jnp.exp(-z))


def _softplus(z):
    return jnp.maximum(z, 0.0) + jnp.log(1.0 + jnp.exp(-jnp.abs(z)))


def _silu(z):
    return z * _sigmoid(z)


def _rms_scale(x):
    return lax.rsqrt(jnp.mean(x * x, axis=-1, keepdims=True) + EPS)


def _rms_bwd(x, w, g):
    r = _rms_scale(x)
    gw = g * w
    dx = r * gw - x * (r * r * r) * jnp.mean(gw * x, axis=-1, keepdims=True)
    return dx, g * x * r


def _matmul(a, b, *, name, ta=False, tb=False, tm=512, tn=512, tk=512, out_dtypes=(F32,), b3=False, o3=False,
            extra=(), epilogue=None):
    m, k = (a.shape[1], a.shape[0]) if ta else a.shape
    if b3:
        n = b.shape[1] if tb else b.shape[0] * b.shape[2]
        kb = b.shape[0] * b.shape[2] if tb else b.shape[1]
    else:
        n, kb = (b.shape[0], b.shape[1]) if tb else (b.shape[1], b.shape[0])
    assert kb == k, (name, kb, k)
    tm, tn, tk = min(tm, m), min(tn, n), min(tk, k)
    assert m % tm == 0 and n % tn == 0 and k % tk == 0, (name, m, n, k, tm, tn, tk)
    nk = k // tk
    n_extra = len(extra)
    n_out = len(out_dtypes)

    def body(*refs):
        a_ref, b_ref = refs[0], refs[1]
        extra_refs = refs[2:2 + n_extra]
        out_refs = refs[2 + n_extra:2 + n_extra + n_out]
        acc_ref = refs[-1]
        kk = pl.program_id(2)

        @pl.when(kk == 0)
        def _():
            acc_ref[...] = jnp.zeros_like(acc_ref)

        acc_ref[...] += _dot(a_ref[...], b_ref[...], 0 if ta else 1, 1 if tb else 0)

        @pl.when(kk == nk - 1)
        def _():
            acc = acc_ref[...]
            outs = (acc,) if epilogue is None else epilogue(acc, *[r[...] for r in extra_refs])
            for o_ref, val in zip(out_refs, outs):
                o_ref[...] = val.astype(o_ref.dtype)

    a_spec = pl.BlockSpec((tk, tm), lambda i, j, kk: (kk, i)) if ta else pl.BlockSpec((tm, tk), lambda i, j, kk: (i, kk))
    if b3 and tb:
        assert b.shape[2] == tk
        b_spec = pl.BlockSpec((None, tn, tk), lambda i, j, kk: (kk, j, 0))
    elif b3:
        assert b.shape[2] == tn
        b_spec = pl.BlockSpec((None, tk, tn), lambda i, j, kk: (j, kk, 0))
    elif tb:
        b_spec = pl.BlockSpec((tn, tk), lambda i, j, kk: (j, kk))
    else:
        b_spec = pl.BlockSpec((tk, tn), lambda i, j, kk: (kk, j))
    tile = pl.BlockSpec((tm, tn), lambda i, j, kk: (i, j))
    out_specs = [tile] * n_out
    out_shape = [jax.ShapeDtypeStruct((m, n), dt) for dt in out_dtypes]
    if o3:
        out_specs[0] = pl.BlockSpec((None, tm, tn), lambda i, j, kk: (j, i, 0))
        out_shape[0] = jax.ShapeDtypeStruct((n // tn, m, tn), out_dtypes[0])
    res = pl.pallas_call(
        body, name=name, grid=(m // tm, n // tn, nk),
        in_specs=[a_spec, b_spec] + [tile] * n_extra, out_specs=out_specs, out_shape=out_shape,
        scratch_shapes=[pltpu.VMEM((tm, tn), F32)],
        compiler_params=_cparams(dimension_semantics=("parallel", "parallel", "arbitrary")),
    )(a, b, *extra)
    return res[0] if n_out == 1 else res


TR = 256


def _row_spec(cols):
    return pl.BlockSpec((TR, cols), lambda i: (i, 0))


def _vec_spec(cols):
    return pl.BlockSpec((1, cols), lambda i: (0, 0))


def _pre_norm(x, w):
    def body(x_ref, w_ref, h_ref):
        xv = x_ref[...]
        h_ref[...] = (xv * _rms_scale(xv) * w_ref[...]).astype(BF16)

    return pl.pallas_call(
        body, name="pre_norm", grid=(S // TR,), in_specs=[_row_spec(D), _vec_spec(D)], out_specs=_row_spec(D),
        out_shape=jax.ShapeDtypeStruct((S, D), BF16), compiler_params=_cparams(),
    )(x, w)


def _post_mix(x, mixed, w_post, w_pre_mlp):
    def body(x_ref, m_ref, wp_ref, wm_ref, x1_ref, h2_ref):
        mv = m_ref[...]
        x1 = x_ref[...] + mv * _rms_scale(mv) * wp_ref[...]
        x1_ref[...] = x1
        h2_ref[...] = (x1 * _rms_scale(x1) * wm_ref[...]).astype(BF16)

    return pl.pallas_call(
        body, name="post_mix", grid=(S // TR,),
        in_specs=[_row_spec(D), _row_spec(D), _vec_spec(D), _vec_spec(D)], out_specs=[_row_spec(D), _row_spec(D)],
        out_shape=[jax.ShapeDtypeStruct((S, D), F32), jax.ShapeDtypeStruct((S, D), BF16)], compiler_params=_cparams(),
    )(x, mixed, w_post, w_pre_mlp)


def _loss_head(x1, y, w_post_mlp, target):
    def body(x1_ref, y_ref, w_ref, t_ref, dx2_ref, dy_ref, dw_ref, loss_ref):
        i = pl.program_id(0)
        yv = y_ref[...]
        w = w_ref[...]
        x2 = x1_ref[...] + yv * _rms_scale(yv) * w
        err = x2 - t_ref[...]
        dx2 = err * (1.0 / D)
        dx2_ref[...] = dx2
        dy, dwt = _rms_bwd(yv, w, dx2)
        dy_ref[...] = dy.astype(BF16)

        @pl.when(i == 0)
        def _():
            dw_ref[...] = jnp.zeros_like(dw_ref)
            loss_ref[...] = jnp.zeros_like(loss_ref)

        dw_ref[...] += jnp.sum(dwt, axis=0, keepdims=True)
        part = 0.5 * jnp.sum(jnp.mean(err * err, axis=-1, keepdims=True), axis=0, keepdims=True)
        loss_ref[...] += jnp.broadcast_to(part, loss_ref.shape)

    return pl.pallas_call(
        body, name="loss_head", grid=(S // TR,),
        in_specs=[_row_spec(D), _row_spec(D), _vec_spec(D), _row_spec(D)],
        out_specs=[_row_spec(D), _row_spec(D), _vec_spec(D), _vec_spec(LANES)],
        out_shape=[jax.ShapeDtypeStruct((S, D), F32), jax.ShapeDtypeStruct((S, D), BF16),
                   jax.ShapeDtypeStruct((1, D), F32), jax.ShapeDtypeStruct((1, LANES), F32)],
        compiler_params=_cparams(),
    )(x1, y, w_post_mlp, target)


def _mid_bwd(dh2, x1, w_pre_mlp, dx2, mixed, w_post):
    def body(dh2_ref, x1_ref, wm_ref, dx2_ref, m_ref, wp_ref, dx1_ref, dm_ref, dwm_ref, dwp_ref):
        i = pl.program_id(0)
        dxa, dwm = _rms_bwd(x1_ref[...], wm_ref[...], dh2_ref[...])
        dx1 = dx2_ref[...] + dxa
        dx1_ref[...] = dx1
        dm, dwp = _rms_bwd(m_ref[...], wp_ref[...], dx1)
        dm_ref[...] = dm.astype(BF16)

        @pl.when(i == 0)
        def _():
            dwm_ref[...] = jnp.zeros_like(dwm_ref)
            dwp_ref[...] = jnp.zeros_like(dwp_ref)

        dwm_ref[...] += jnp.sum(dwm, axis=0, keepdims=True)
        dwp_ref[...] += jnp.sum(dwp, axis=0, keepdims=True)

    return pl.pallas_call(
        body, name="mid_bwd", grid=(S // TR,),
        in_specs=[_row_spec(D), _row_spec(D), _vec_spec(D), _row_spec(D), _row_spec(D), _vec_spec(D)],
        out_specs=[_row_spec(D), _row_spec(D), _vec_spec(D), _vec_spec(D)],
        out_shape=[jax.ShapeDtypeStruct((S, D), F32), jax.ShapeDtypeStruct((S, D), BF16),
                   jax.ShapeDtypeStruct((1, D), F32), jax.ShapeDtypeStruct((1, D), F32)],
        compiler_params=_cparams(),
    )(dh2, x1, w_pre_mlp, dx2, mixed, w_post)


def _pre_norm_bwd(dh, x, w, dx1):
    def body(dh_ref, x_ref, w_ref, dx1_ref, dx_ref, dw_ref):
        i = pl.program_id(0)
        dxa, dwt = _rms_bwd(x_ref[...], w_ref[...], dh_ref[...])
        dx_ref[...] = dx1_ref[...] + dxa

        @pl.when(i == 0)
        def _():
            dw_ref[...] = jnp.zeros_like(dw_ref)

        dw_ref[...] += jnp.sum(dwt, axis=0, keepdims=True)

    return pl.pallas_call(
        body, name="pre_norm_bwd", grid=(S // TR,),
        in_specs=[_row_spec(D), _row_spec(D), _vec_spec(D), _row_spec(D)], out_specs=[_row_spec(D), _vec_spec(D)],
        out_shape=[jax.ShapeDtypeStruct((S, D), F32), jax.ShapeDtypeStruct((1, D), F32)], compiler_params=_cparams(),
    )(dh, x, w, dx1)


BQ = 256
NQ = S // BQ
LANE_BETA, LANE_G = 8, 12


def _gate_lanes(shape):
    lane = lax.broadcasted_iota(jnp.int32, shape, 1)
    return lane < LANE_BETA, (lane >= LANE_BETA) & (lane < LANE_G), (lane >= LANE_G) & (lane < LANE_G + NGH)


def _gates(proj, bias_vec, alog_vec):
    def body(s_ref, b_ref, a_ref, o_ref, t_ref, carry_ref):
        i = pl.program_id(0)

        @pl.when(i == 0)
        def _():
            carry_ref[...] = jnp.zeros_like(carry_ref)

        z = s_ref[...] + b_ref[...]
        tail = jnp.log(1.0 + jnp.exp(-jnp.abs(z)))
        sp = jnp.maximum(z, 0.0) + tail
        lf = jnp.minimum(z, 0.0) - tail
        r = lax.broadcasted_iota(jnp.int32, (BQ, BQ), 0)
        c = lax.broadcasted_iota(jnp.int32, (BQ, BQ), 1)
        tri = (c <= r).astype(F32)
        cum = _hdot(tri, lf) + carry_ref[...]
        carry_ref[...] = cum[BQ - 1:BQ, :]
        is_fox, is_beta, is_g = _gate_lanes(z.shape)
        out = jnp.where(is_fox, cum, jnp.where(is_beta, _sigmoid(z), jnp.where(is_g, -jnp.exp(a_ref[...]) * sp, 0.0)))
        o_ref[...] = out
        t_ref[...] = out.T

    return pl.pallas_call(
        body, name="gates", grid=(NQ,),
        in_specs=[pl.BlockSpec((BQ, LANES), lambda i: (i, BLK_SMALL)), _vec_spec(LANES), _vec_spec(LANES)],
        out_specs=[pl.BlockSpec((BQ, LANES), lambda i: (i, 0)), pl.BlockSpec((None, LANES, BQ), lambda i: (i, 0, 0))],
        out_shape=[jax.ShapeDtypeStruct((S, LANES), F32), jax.ShapeDtypeStruct((NQ, LANES, BQ), F32)],
        scratch_shapes=[pltpu.VMEM((1, LANES), F32)], compiler_params=_cparams(),
    )(proj, bias_vec, alog_vec)


def _gates_bwd(proj, bias_vec, alog_vec, dgates_gdn, dcum_fox, dproj):
    def body(s_ref, b_ref, a_ref, dg_ref, dc_ref, dproj_in, dproj_ref, red_ref, carry_ref):
        del dproj_in
        i = pl.program_id(0)

        @pl.when(i == 0)
        def _():
            carry_ref[...] = jnp.zeros_like(carry_ref)
            red_ref[...] = jnp.zeros_like(red_ref)

        z = s_ref[...] + b_ref[...]
        dg = dg_ref[...] + dc_ref[...]
        r = lax.broadcasted_iota(jnp.int32, (BQ, BQ), 0)
        c = lax.broadcasted_iota(jnp.int32, (BQ, BQ), 1)
        upper = (c >= r).astype(F32)
        dlf = _hdot(upper, dg) + carry_ref[...]
        carry_ref[...] = dlf[0:1, :]
        sig = _sigmoid(z)
        g_scale = -jnp.exp(a_ref[...])
        is_fox, is_beta, is_g = _gate_lanes(z.shape)
        ds = jnp.where(is_fox, dlf * (1.0 - sig), jnp.where(is_beta, dg * sig * (1.0 - sig), jnp.where(is_g, dg * g_scale * sig, 0.0)))
        dproj_ref[:, 0:LANES] = ds.astype(BF16)
        dproj_ref[:, LANES:2 * LANES] = jnp.zeros((BQ, LANES), BF16)
        dalog = jnp.where(is_g, dg * g_scale * _softplus(z), 0.0)
        sums = jnp.sum(ds, axis=0, keepdims=True)
        red_ref[0:1, :] += jnp.where(is_fox[0:1], sums, 0.0)
        red_ref[1:2, :] += pltpu.roll(jnp.where(is_g[0:1], sums, 0.0), LANES - LANE_G, 1)
        red_ref[2:3, :] += pltpu.roll(jnp.sum(dalog, axis=0, keepdims=True), LANES - LANE_G, 1)

    blk = pl.BlockSpec((BQ, LANES), lambda i: (NQ - 1 - i, 0))
    return pl.pallas_call(
        body, name="gates_bwd", grid=(NQ,),
        in_specs=[pl.BlockSpec((BQ, LANES), lambda i: (NQ - 1 - i, BLK_SMALL)), _vec_spec(LANES), _vec_spec(LANES), blk, blk,
                  pl.BlockSpec(memory_space=pl.ANY)],
        out_specs=[pl.BlockSpec((BQ, 2 * LANES), lambda i: (NQ - 1 - i, BLK_SMALL // 2)), pl.BlockSpec((8, LANES), lambda i: (0, 0))],
        out_shape=[jax.ShapeDtypeStruct((S, DPROJ_PAD), BF16), jax.ShapeDtypeStruct((8, LANES), F32)],
        input_output_aliases={5: 0},
        scratch_shapes=[pltpu.VMEM((1, LANES), F32)], compiler_params=_cparams(),
    )(proj, bias_vec, alog_vec, dgates_gdn, dcum_fox, dproj)


FOX_SCALE = FHD ** -0.5


def _head_mask(e):
    lane = lax.broadcasted_iota(jnp.int32, (1, LANES), 1)
    return (lane >= e * FHD) & (lane < (e + 1) * FHD)


def _lane_col(vals, index):
    lane = lax.broadcasted_iota(jnp.int32, vals.shape, 1)
    return jnp.sum(jnp.where(lane == index, vals, 0.0), axis=1, keepdims=True)


def _sublane_row(vals, index):
    row = lax.broadcasted_iota(jnp.int32, vals.shape, 0)
    return jnp.sum(jnp.where(row == index, vals, 0.0), axis=0, keepdims=True)


def _pair_cols(c0, c1):
    lane = lax.broadcasted_iota(jnp.int32, (c0.shape[0], 2), 1)
    return jnp.where(lane == 0, c0, c1)


def _fox_logits(q, k, cum_q, cum_k, qi, kj):
    s = _dot(q, k, 1, 1) * FOX_SCALE + (cum_q - cum_k)
    row = qi * BQ + lax.broadcasted_iota(jnp.int32, (BQ, BQ), 0)
    col = kj * BQ + lax.broadcasted_iota(jnp.int32, (BQ, BQ), 1)
    return jnp.where(col <= row, s, -jnp.inf)


def _head_rms(o, masks):
    o2 = o * o
    r = [lax.rsqrt(jnp.sum(jnp.where(mk, o2, 0.0), axis=1, keepdims=True) * (1.0 / FHD) + EPS) for mk in masks]
    return jnp.where(masks[0], r[0], r[1])


def _fox_fwd(proj, gates, gates_t, w2):
    def body(q_ref, k_ref, v_ref, g_ref, ct_ref, w_ref, mix_ref, o_ref, lse_ref, kb_ref, vb_ref):
        hp, qi = pl.program_id(0), pl.program_id(1)

        @pl.when(qi == 0)
        def _():
            kb_ref[...] = k_ref[...].astype(BF16)
            vb_ref[...] = v_ref[...].astype(BF16)

        masks = [_head_mask(0), _head_mask(1)]
        qblk = q_ref[...]
        gt = g_ref[...]
        qs = [jnp.where(mk, qblk, 0.0).astype(BF16) for mk in masks]
        cqs = [_lane_col(gt, 2 * hp + e) for e in range(2)]

        def step(kj, carry):
            rows = pl.ds(pl.multiple_of(kj * BQ, BQ), BQ)
            kb = kb_ref[rows, :]
            vb = vb_ref[rows, :]
            ctk = ct_ref[kj]
            new = []
            for e in range(2):
                m, l, acc = carry[e]
                s = _fox_logits(qs[e], kb, cqs[e], _sublane_row(ctk, 2 * hp + e), qi, kj)
                m_new = jnp.maximum(m, jnp.max(s, axis=-1, keepdims=True))
                p = jnp.exp(s - m_new)
                alpha = jnp.exp(m - m_new)
                new.append((m_new, alpha * l + jnp.sum(p, axis=-1, keepdims=True), alpha * acc + _dot(p, vb)))
            return tuple(new)

        one = (jnp.full((BQ, 1), -jnp.inf, F32), jnp.zeros((BQ, 1), F32), jnp.zeros((BQ, LANES), F32))
        (m0, l0, a0), (m1, l1, a1) = lax.fori_loop(0, qi + 1, step, (one, one))
        o = jnp.where(masks[0], a0 / l0, a1 / l1)
        o_ref[...] = o
        mix_ref[...] = (o * _head_rms(o, masks) * w_ref[...]).astype(BF16)
        lse_ref[...] = _pair_cols(m0 + jnp.log(l0), m1 + jnp.log(l1))

    def col(part):
        return pl.BlockSpec((S, LANES), lambda hp, i: (0, 3 * hp + part))

    blk = pl.BlockSpec((BQ, LANES), lambda hp, i: (i, hp))
    return pl.pallas_call(
        body, name="fox_fwd", grid=(NPAIR, NQ),
        in_specs=[pl.BlockSpec((BQ, LANES), lambda hp, i: (i, 3 * hp)), col(1), col(2),
                  pl.BlockSpec((BQ, LANES), lambda hp, i: (i, 0)), pl.BlockSpec((NQ, 8, BQ), lambda hp, i: (0, 0, 0)),
                  pl.BlockSpec((1, LANES), lambda hp, i: (0, 0))],
        out_specs=[blk, blk, pl.BlockSpec((None, BQ, 2), lambda hp, i: (hp, i, 0))],
        out_shape=[jax.ShapeDtypeStruct((S, D), BF16), jax.ShapeDtypeStruct((S, DFOX), F32),
                   jax.ShapeDtypeStruct((NPAIR, S, 2), F32)],
        scratch_shapes=[pltpu.VMEM((S, LANES), BF16), pltpu.VMEM((S, LANES), BF16)],
        compiler_params=_cparams(),
    )(proj, proj, proj, gates, gates_t, w2)


def _fox_norm_bwd(o, dmix, w2):
    def body(o_ref, g_ref, w_ref, do_ref, dl_ref, dw_ref):
        hp, qi = pl.program_id(0), pl.program_id(1)
        masks = [_head_mask(0), _head_mask(1)]
        ov = o_ref[...]
        g = g_ref[...]
        r = _head_rms(ov, masks)
        gw = g * w_ref[...]
        gwo = gw * ov
        mean = [jnp.sum(jnp.where(mk, gwo, 0.0), axis=1, keepdims=True) * (1.0 / FHD) for mk in masks]
        do = r * gw - ov * (r * r * r) * jnp.where(masks[0], mean[0], mean[1])
        do_ref[...] = do.astype(BF16)
        doo = do * ov
        dl_ref[...] = _pair_cols(*[jnp.sum(jnp.where(mk, doo, 0.0), axis=1, keepdims=True) for mk in masks])

        @pl.when((hp == 0) & (qi == 0))
        def _():
            dw_ref[...] = jnp.zeros_like(dw_ref)

        dw_ref[...] += jnp.sum(g * ov * r, axis=0, keepdims=True)

        @pl.when((hp == NPAIR - 1) & (qi == NQ - 1))
        def _():
            dw = dw_ref[...]
            dw_ref[...] = dw + pltpu.roll(dw, FHD, 1)

    blk = pl.BlockSpec((BQ, LANES), lambda hp, i: (i, hp))
    vec = pl.BlockSpec((1, LANES), lambda hp, i: (0, 0))
    return pl.pallas_call(
        body, name="fox_norm_bwd", grid=(NPAIR, NQ), in_specs=[blk, blk, vec],
        out_specs=[blk, pl.BlockSpec((None, BQ, 2), lambda hp, i: (hp, i, 0)), vec],
        out_shape=[jax.ShapeDtypeStruct((S, DFOX), BF16), jax.ShapeDtypeStruct((NPAIR, S, 2), F32),
                   jax.ShapeDtypeStruct((1, LANES), F32)],
        compiler_params=_cparams(),
    )(o, dmix, w2)


def _fox_bwd(proj, do, gates, gates_t, lse, delta):
    def body(q_ref, k_ref, v_ref, do_ref, g_ref, ct_ref, lse_ref, dl_ref, dproj_ref, dc_ref, qb_ref, dq_ref):
        hp, kj = pl.program_id(0), pl.program_id(1)

        @pl.when(kj == 0)
        def _():
            qb_ref[...] = q_ref[...].astype(BF16)
            dq_ref[...] = jnp.zeros_like(dq_ref)

        @pl.when((hp == 0) & (kj == 0))
        def _():
            dc_ref[...] = jnp.zeros_like(dc_ref)

        masks = [_head_mask(0), _head_mask(1)]
        kb = k_ref[...].astype(BF16)
        vb = v_ref[...].astype(BF16)
        ctk = ct_ref[...]
        cks = [_sublane_row(ctk, 2 * hp + e) for e in range(2)]
        lane = lax.broadcasted_iota(jnp.int32, (BQ, LANES), 1)

        def step(qi, carry):
            dk, dv, cs = carry
            rows = pl.ds(pl.multiple_of(qi * BQ, BQ), BQ)
            qv = qb_ref[rows, :]
            dov = do_ref[rows, :]
            gt = g_ref[rows, :]
            lse2 = lse_ref[rows, :]
            dl2 = dl_ref[rows, :]
            dq = jnp.zeros((BQ, LANES), F32)
            dc = jnp.zeros((BQ, LANES), F32)
            cs_new = []
            for e in range(2):
                h = 2 * hp + e
                qe = jnp.where(masks[e], qv, jnp.zeros_like(qv))
                doe = jnp.where(masks[e], dov, jnp.zeros_like(dov))
                s = _fox_logits(qe, kb, _lane_col(gt, h), cks[e], qi, kj)
                p = jnp.exp(s - _lane_col(lse2, e))
                dp = _dot(doe, vb, 1, 1)
                ds = p * (dp - _lane_col(dl2, e))
                dv = dv + _dot(p, doe, 0, 0)
                dk = dk + _dot(ds, qe, 0, 0) * FOX_SCALE
                dq = dq + jnp.where(masks[e], _dot(ds, kb), 0.0) * FOX_SCALE
                cs_new.append(cs[e] + jnp.sum(ds, axis=0, keepdims=True))
                dc = dc + jnp.where(lane == h, jnp.sum(ds, axis=1, keepdims=True), 0.0)
            dq_ref[rows, :] += dq
            dc_ref[rows, :] += dc
            return dk, dv, tuple(cs_new)

        zero = jnp.zeros((BQ, LANES), F32)
        dk, dv, cs = lax.fori_loop(kj, NQ, step, (zero, zero, (jnp.zeros((1, BQ), F32), jnp.zeros((1, BQ), F32))))
        krows = pl.ds(pl.multiple_of(kj * BQ, BQ), BQ)
        dproj_ref[krows, LANES:2 * LANES] = dk.astype(BF16)
        dproj_ref[krows, 2 * LANES:3 * LANES] = dv.astype(BF16)
        r = lax.broadcasted_iota(jnp.int32, (BQ, BQ), 0)
        c = lax.broadcasted_iota(jnp.int32, (BQ, BQ), 1)
        dcol = jnp.zeros((BQ, LANES), F32)
        for e in range(2):
            col = jnp.sum(jnp.where(r == c, cs[e], 0.0), axis=1, keepdims=True)
            dcol = dcol + jnp.where(lane == 2 * hp + e, col, 0.0)
        dc_ref[krows, :] -= dcol

        @pl.when(kj == NQ - 1)
        def _():
            dproj_ref[:, 0:LANES] = dq_ref[...].astype(BF16)

    def blk(part):
        return pl.BlockSpec((BQ, LANES), lambda hp, j: (j, 3 * hp + part))

    pair = pl.BlockSpec((None, S, 2), lambda hp, j: (hp, 0, 0))
    return pl.pallas_call(
        body, name="fox_bwd", grid=(NPAIR, NQ),
        in_specs=[pl.BlockSpec((S, LANES), lambda hp, j: (0, 3 * hp)), blk(1), blk(2),
                  pl.BlockSpec((S, LANES), lambda hp, j: (0, hp)), pl.BlockSpec((S, LANES), lambda hp, j: (0, 0)),
                  pl.BlockSpec((None, 8, BQ), lambda hp, j: (j, 0, 0)), pair, pair],
        out_specs=[pl.BlockSpec((S, 3 * LANES), lambda hp, j: (0, hp)), pl.BlockSpec((S, LANES), lambda hp, j: (0, 0))],
        out_shape=[jax.ShapeDtypeStruct((S, DPROJ_PAD), BF16), jax.ShapeDtypeStruct((S, LANES), F32)],
        scratch_shapes=[pltpu.VMEM((S, LANES), BF16), pltpu.VMEM((S, LANES), F32)],
        compiler_params=_cparams(),
    )(proj, proj, proj, do, gates, gates_t, lse, delta)


NQKV = 3 * NGH
GDN_QSCALE = GHD ** -0.5


def _shift_down(x, s):
    if s == 0:
        return x
    row = lax.broadcasted_iota(jnp.int32, x.shape, 0)
    return jnp.where(row >= s, pltpu.roll(x, s, 0), 0.0)


def _shift_up(x, s):
    if s == 0:
        return x
    n = x.shape[0]
    row = lax.broadcasted_iota(jnp.int32, x.shape, 0)
    return jnp.where(row < n - s, pltpu.roll(x, n - s, 0), 0.0)


def _conv_pre(xv, wv):
    pre = xv * wv[CONV_K - 1:CONV_K, :]
    for j in range(CONV_K - 1):
        pre = pre + _shift_down(xv, CONV_K - 1 - j) * wv[j:j + 1, :]
    return pre


def _l2_factors(b):
    return b < 2 * NGH, jnp.where(b < NGH, GDN_QSCALE, 1.0)


def _gdn_pre(proj, conv_w):
    def body(x_ref, w_ref, o_ref):
        b = pl.program_id(0)
        c = _silu(_conv_pre(x_ref[...], w_ref[...]))
        normed, scale = _l2_factors(b)
        rs = lax.rsqrt(jnp.sum(c * c, axis=-1, keepdims=True) + EPS)
        o_ref[...] = c * jnp.where(normed, rs, 1.0) * scale

    return pl.pallas_call(
        body, name="gdn_pre", grid=(NQKV,),
        in_specs=[pl.BlockSpec((S, GHD), lambda b: (0, BLK_GDN + b)), pl.BlockSpec((CONV_K, GHD), lambda b: (0, b))],
        out_specs=pl.BlockSpec((S, GHD), lambda b: (0, b)),
        out_shape=jax.ShapeDtypeStruct((S, NQKV * GHD), F32), compiler_params=_cparams(),
    )(proj, conv_w)


def _gdn_pre_bwd(proj, conv_w, dqkv, dproj):
    def body(x_ref, w_ref, dy_ref, dproj_in, dx_ref, dw_ref):
        del dproj_in
        b = pl.program_id(0)
        xv = x_ref[...]
        wv = w_ref[...]
        pre = _conv_pre(xv, wv)
        sig = _sigmoid(pre)
        c = pre * sig
        normed, scale = _l2_factors(b)
        g = dy_ref[...] * scale
        rs = lax.rsqrt(jnp.sum(c * c, axis=-1, keepdims=True) + EPS)
        dc_n = rs * g - c * (rs * rs * rs) * jnp.sum(g * c, axis=-1, keepdims=True)
        dc = jnp.where(normed, dc_n, g)
        dpre = dc * sig * (1.0 + pre * (1.0 - sig))
        dx = dpre * wv[CONV_K - 1:CONV_K, :]
        for j in range(CONV_K - 1):
            dx = dx + _shift_up(dpre, CONV_K - 1 - j) * wv[j:j + 1, :]
        dx_ref[...] = dx.astype(BF16)
        for j in range(CONV_K):
            dw_ref[j:j + 1, :] = jnp.sum(dpre * _shift_down(xv, CONV_K - 1 - j), axis=0, keepdims=True)

    return pl.pallas_call(
        body, name="gdn_pre_bwd", grid=(NQKV,),
        in_specs=[pl.BlockSpec((S, GHD), lambda b: (0, BLK_GDN + b)), pl.BlockSpec((CONV_K, GHD), lambda b: (0, b)),
                  pl.BlockSpec((None, S, GHD), lambda b: (b // NGH, 0, b % NGH)), pl.BlockSpec(memory_space=pl.ANY)],
        out_specs=[pl.BlockSpec((S, GHD), lambda b: (0, BLK_GDN + b)), pl.BlockSpec((CONV_K, GHD), lambda b: (0, b))],
        out_shape=[jax.ShapeDtypeStruct((S, DPROJ_PAD), BF16), jax.ShapeDtypeStruct((CONV_K, NQKV * GHD), F32)],
        input_output_aliases={3: 0}, compiler_params=_cparams(),
    )(proj, conv_w, dqkv, dproj)


CB = 4
NCB = NCH // CB


def _chunk_prep(qs, ks, vs, gcols, bcols, t_saved=None):
    n = range(len(qs))
    r = lax.broadcasted_iota(jnp.int32, (CHUNK, CHUNK), 0)
    c = lax.broadcasted_iota(jnp.int32, (CHUNK, CHUNK), 1)
    incl = c <= r
    eye = (r == c).astype(F32)
    grow = [jnp.sum(gcols[i] * eye, axis=0, keepdims=True) for i in n]
    gc_col = [jnp.sum(jnp.where(incl, grow[i], 0.0), axis=1, keepdims=True) for i in n]
    gc_row = [jnp.sum(jnp.where(r <= c, gcols[i], 0.0), axis=0, keepdims=True) for i in n]
    decay = [jnp.exp(jnp.where(incl, gc_col[i] - gc_row[i], -jnp.inf)) for i in n]
    kb = [ks[i] * bcols[i] for i in n]
    vb = [vs[i] * bcols[i] for i in n]
    kk = [_mm_nt(kb[i], ks[i]) for i in n]
    m = [jnp.where(c < r, kk[i] * decay[i], 0.0) for i in n]
    if t_saved is None:
        t_inv = [eye - m[i] for i in n]
        p = [_hdot(m[i], m[i]) for i in n]
        for step in range(5):
            t_inv = [t_inv[i] + _hdot(t_inv[i], p[i]) for i in n]
            if step < 4:
                p = [_hdot(p[i], p[i]) for i in n]
    else:
        t_inv = [_saved_inverse(m[i], t_saved[i]) for i in n]
    egc = [jnp.exp(gc_col[i]) for i in n]
    u = [_mm_nn(t_inv[i], vb[i]) for i in n]
    w = [_mm_nn(t_inv[i], kb[i] * egc[i]) for i in n]
    qk = [_mm_nt(qs[i], ks[i]) for i in n]
    gc_last = [gc_col[i][CHUNK - 1:CHUNK, :] for i in n]
    return [(u[i], w[i], qk[i] * decay[i], qs[i] * egc[i], ks[i] * jnp.exp(gc_last[i] - gc_col[i]), jnp.exp(gc_last[i]),
             t_inv[i]) for i in n]


def _prep_specs():
    rows = CB * CHUNK
    qs = pl.BlockSpec((rows, GHD), lambda i, h: (i, h))
    ks = pl.BlockSpec((rows, GHD), lambda i, h: (i, NGH + h))
    vs = pl.BlockSpec((rows, GHD), lambda i, h: (i, 2 * NGH + h))
    gs = pl.BlockSpec((rows, LANES), lambda i, h: (i, 0))
    a_s = pl.BlockSpec((None, rows, CHUNK), lambda i, h: (h, i, 0))
    gl_s = pl.BlockSpec((None, CB, 1, LANES), lambda i, h: (h, i, 0, 0))
    return qs, ks, vs, gs, a_s, gl_s


def _gdn_prep(qkv, gates):
    def body(q_ref, k_ref, v_ref, g_ref, u_ref, w_ref, qd_ref, kd_ref, a_ref, gl_ref, t_ref):
        h = pl.program_id(1)
        chunks = [pl.ds(cidx * CHUNK, CHUNK) for cidx in range(CB)]
        gts = [g_ref[rows, :] for rows in chunks]
        outs = _chunk_prep([q_ref[rows, :] for rows in chunks], [k_ref[rows, :] for rows in chunks],
                           [v_ref[rows, :] for rows in chunks], [_lane_col(gt, LANE_G + h) for gt in gts],
                           [_lane_col(gt, LANE_BETA + h) for gt in gts])
        for cidx, rows in enumerate(chunks):
            u, w, a, qd, kd, gl, t_inv = outs[cidx]
            u_ref[rows, :] = u
            w_ref[rows, :] = w
            qd_ref[rows, :] = qd
            kd_ref[rows, :] = kd
            a_ref[rows, :] = a
            t_ref[rows, :] = t_inv
            gl_ref[cidx] = jnp.broadcast_to(gl, (1, LANES))

    qs, ks, vs, gs, a_s, gl_s = _prep_specs()
    tok = jax.ShapeDtypeStruct((S, DGDN), F32)
    sq = jax.ShapeDtypeStruct((NGH, S, CHUNK), F32)
    return pl.pallas_call(
        body, name="gdn_prep", grid=(NCB, NGH), in_specs=[qs, ks, vs, gs], out_specs=[qs, qs, qs, qs, a_s, gl_s, a_s],
        out_shape=[tok, tok, tok, tok, sq, jax.ShapeDtypeStruct((NGH, NCH, 1, LANES), F32), sq],
        compiler_params=_cparams(),
    )(qkv, qkv, qkv, gates)


def _gdn_prep_bwd(qkv, gates, t_inv, du, dw, dqd, dkd, da, dgl):
    def body(q_ref, k_ref, v_ref, g_ref, t_ref, du_ref, dw_ref, dqd_ref, dkd_ref, da_ref, dgl_ref, dqkv_ref, dg_ref):
        h = pl.program_id(1)

        @pl.when(h == 0)
        def _():
            dg_ref[...] = jnp.zeros_like(dg_ref)

        lane = lax.broadcasted_iota(jnp.int32, (CHUNK, LANES), 1)
        chunks = [pl.ds(cidx * CHUNK, CHUNK) for cidx in range(CB)]
        gts = [g_ref[rows, :] for rows in chunks]
        t_saved = [t_ref[rows, :] for rows in chunks]
        _, vjp = jax.vjp(lambda *args: [o[:6] for o in _chunk_prep(*args, t_saved=t_saved)],
                         [q_ref[rows, :] for rows in chunks], [k_ref[rows, :] for rows in chunks],
                         [v_ref[rows, :] for rows in chunks], [_lane_col(gt, LANE_G + h) for gt in gts],
                         [_lane_col(gt, LANE_BETA + h) for gt in gts])
        dqs, dks, dvs, dgcs, dbcs = vjp([(du_ref[rows, :], dw_ref[rows, :], da_ref[rows, :], dqd_ref[rows, :],
                                          dkd_ref[rows, :], dgl_ref[cidx][:, 0:1]) for cidx, rows in enumerate(chunks)])
        for cidx, rows in enumerate(chunks):
            dq, dk, dv, dgc, dbc = dqs[cidx], dks[cidx], dvs[cidx], dgcs[cidx], dbcs[cidx]
            dqkv_ref[0, rows, :] = dq
            dqkv_ref[1, rows, :] = dk
            dqkv_ref[2, rows, :] = dv
            dg_ref[rows, :] += jnp.where(lane == LANE_G + h, dgc, 0.0) + jnp.where(lane == LANE_BETA + h, dbc, 0.0)

    qs, ks, vs, gs, a_s, gl_s = _prep_specs()
    return pl.pallas_call(
        body, name="gdn_prep_bwd", grid=(NCB, NGH), in_specs=[qs, ks, vs, gs, a_s, qs, qs, qs, qs, a_s, gl_s],
        out_specs=[pl.BlockSpec((3, CB * CHUNK, GHD), lambda i, h: (0, i, h)), gs],
        out_shape=[jax.ShapeDtypeStruct((3, S, DGDN), F32), jax.ShapeDtypeStruct((S, LANES), F32)],
        compiler_params=_cparams(),
    )(qkv, qkv, qkv, gates, t_inv, du, dw, dqd, dkd, da, dgl)


def _scan_specs():
    hs = pl.BlockSpec((S, GHD), lambda h: (0, h))
    a_s = pl.BlockSpec((None, S, CHUNK), lambda h: (h, 0, 0))
    gl_s = pl.BlockSpec((None, NCH, 1, LANES), lambda h: (h, 0, 0, 0))
    st_s = pl.BlockSpec((None, NCH, GHD, GHD), lambda h: (h, 0, 0, 0))
    gz_s = pl.BlockSpec((S, GHD), lambda h: (0, BLK_GZ + h))
    mix_s = pl.BlockSpec((S, GHD), lambda h: (0, NPAIR + h))
    return hs, a_s, gl_s, st_s, gz_s, mix_s


def _gdn_scan(u, w, qd, kd, a, gl, proj, w_norm, mix):
    def body(u_ref, w_ref, qd_ref, kd_ref, a_ref, gl_ref, z_ref, wn_ref, mix_in, mix_ref, o_ref, st_ref):
        del mix_in

        def step(ci, state):
            rows = pl.ds(pl.multiple_of(ci * CHUNK, CHUNK), CHUNK)
            st_ref[ci] = state
            vn = u_ref[rows, :] - _dot(w_ref[rows, :], state)
            o_ref[rows, :] = _dot(qd_ref[rows, :], state) + _dot(a_ref[rows, :], vn)
            return state * gl_ref[ci] + _dot(kd_ref[rows, :], vn, 0, 0)

        lax.fori_loop(0, NCH, step, jnp.zeros((GHD, GHD), F32))
        ov = o_ref[...]
        mix_ref[...] = (ov * _rms_scale(ov) * wn_ref[...] * _silu(z_ref[...])).astype(BF16)

    hs, a_s, gl_s, st_s, gz_s, mix_s = _scan_specs()
    return pl.pallas_call(
        body, name="gdn_scan", grid=(NGH,),
        in_specs=[hs, hs, hs, hs, a_s, gl_s, gz_s, pl.BlockSpec((1, GHD), lambda h: (0, 0)), pl.BlockSpec(memory_space=pl.ANY)],
        out_specs=[mix_s, hs, st_s],
        out_shape=[jax.ShapeDtypeStruct((S, D), BF16), jax.ShapeDtypeStruct((S, DGDN), F32),
                   jax.ShapeDtypeStruct((NGH, NCH, GHD, GHD), F32)],
        input_output_aliases={8: 0}, compiler_params=_cparams(),
    )(u, w, qd, kd, a, gl, proj, w_norm, mix)


def _gdn_scan_bwd(dmix, o, proj, w_norm, u, w, qd, kd, a, gl, states, dproj):
    def body(dy_ref, o_ref, z_ref, wn_ref, u_ref, w_ref, qd_ref, kd_ref, a_ref, gl_ref, st_ref, dproj_in,
             dz_ref, du_ref, dw_ref, dqd_ref, dkd_ref, da_ref, dgl_ref, dwn_ref, do_ref):
        del dproj_in
        h = pl.program_id(0)
        ov = o_ref[...]
        zv = z_ref[...]
        wn = wn_ref[...]
        g = dy_ref[...]
        sig = _sigmoid(zv)
        dz_ref[...] = (g * (ov * _rms_scale(ov) * wn) * sig * (1.0 + zv * (1.0 - sig))).astype(BF16)
        do, dwt = _rms_bwd(ov, wn, g * zv * sig)
        do_ref[...] = do

        @pl.when(h == 0)
        def _():
            dwn_ref[...] = jnp.zeros_like(dwn_ref)

        dwn_ref[...] += jnp.sum(dwt, axis=0, keepdims=True)

        def step(t, dstate):
            ci = NCH - 1 - t
            rows = pl.ds(pl.multiple_of(ci * CHUNK, CHUNK), CHUNK)
            state = st_ref[ci]
            dov = do_ref[rows, :]
            wv = w_ref[rows, :]
            kdv = kd_ref[rows, :]
            vn = u_ref[rows, :] - _dot(wv, state)
            dvn = _dot(a_ref[rows, :], dov, 0, 0) + _dot(kdv, dstate)
            da_ref[rows, :] = _dot(dov, vn, 1, 1)
            dqd_ref[rows, :] = _dot(dov, state, 1, 1)
            dkd_ref[rows, :] = _dot(vn, dstate, 1, 1)
            dgl = jnp.sum(jnp.sum(dstate * state, axis=1, keepdims=True), axis=0, keepdims=True)
            dgl_ref[ci] = jnp.broadcast_to(dgl, (1, LANES))
            du_ref[rows, :] = dvn
            dw_ref[rows, :] = -_dot(dvn, state, 1, 1)
            return dstate * gl_ref[ci] + _dot(qd_ref[rows, :], dov, 0, 0) - _dot(wv, dvn, 0, 0)

        lax.fori_loop(0, NCH, step, jnp.zeros((GHD, GHD), F32))

    hs, a_s, gl_s, st_s, gz_s, mix_s = _scan_specs()
    vec = pl.BlockSpec((1, GHD), lambda h: (0, 0))
    tok = jax.ShapeDtypeStruct((S, DGDN), F32)
    return pl.pallas_call(
        body, name="gdn_scan_bwd", grid=(NGH,),
        in_specs=[mix_s, hs, gz_s, vec, hs, hs, hs, hs, a_s, gl_s, st_s, pl.BlockSpec(memory_space=pl.ANY)],
        out_specs=[gz_s, hs, hs, hs, hs, a_s, gl_s, vec],
        out_shape=[jax.ShapeDtypeStruct((S, DPROJ_PAD), BF16), tok, tok, tok, tok,
                   jax.ShapeDtypeStruct((NGH, S, CHUNK), F32), jax.ShapeDtypeStruct((NGH, NCH, 1, LANES), F32),
                   jax.ShapeDtypeStruct((1, GHD), F32)],
        input_output_aliases={11: 0}, scratch_shapes=[pltpu.VMEM((S, GHD), F32)], compiler_params=_cparams(),
    )(dmix, o, proj, w_norm, u, w, qd, kd, a, gl, states, dproj)


def _place():
    return lax.axis_index("x"), lax.axis_index("y"), lax.axis_index("c")


def _other_chips(x, y):
    return [(1 - x, y), (x, 1 - y), (1 - x, 1 - y)]


HBM = pl.BlockSpec(memory_space=pltpu.HBM)
VMEM = pl.BlockSpec(memory_space=pltpu.VMEM)


def _half_rows(ref_or_rows, half):
    rows = ref_or_rows // 2
    return pl.ds(pl.multiple_of(half * rows, rows), rows)


def _weights_allgather(shards, conv):
    n = len(shards)

    def body(*refs):
        src, conv_src = refs[:n], refs[n]
        outs, conv_out = refs[n + 1:2 * n + 1], refs[2 * n + 1]
        send_sems, recv_sems, local_sems = refs[2 * n + 2:]
        x, y, c = _place()
        me_chip = 2 * x + y
        sibling = (x, y, 1 - c)
        chips = _other_chips(x, y)
        chip_ids = [2 * ch[0] + ch[1] for ch in chips]

        def copy(a, k, chip_index, half, to, from_src):
            rows = _half_rows(src[a].shape[0], half)
            dst = outs[a].at[chip_index, rows]
            return pltpu.make_async_remote_copy(
                src_ref=src[a].at[rows] if from_src else dst, dst_ref=dst, send_sem=send_sems.at[6 * a + k],
                recv_sem=recv_sems.at[6 * a + k], device_id=to, device_id_type=MESH)

        def conv_copy(k, chip_index, to):
            return pltpu.make_async_remote_copy(
                src_ref=conv_src, dst_ref=conv_out.at[chip_index], send_sem=send_sems.at[6 * n + k],
                recv_sem=recv_sems.at[6 * n + k], device_id=to, device_id_type=MESH)

        local = [pltpu.make_async_copy(src[a], outs[a].at[me_chip], local_sems.at[a]) for a in range(n)]
        local.append(pltpu.make_async_copy(conv_src, conv_out.at[me_chip], local_sems.at[n]))
        for cp in local:
            cp.start()
        first = [copy(a, j, me_chip, c, (*chips[j], c), True) for a in range(n) for j in range(3)]
        first += [conv_copy(j, me_chip, (*chips[j], c)) for j in range(3)]
        for cp in first:
            cp.start()
        passed = []
        for a in range(n):
            for j in range(3):
                copy(a, j, chip_ids[j], c, (x, y, c), False).wait_recv()
                cp = copy(a, 3 + j, chip_ids[j], c, sibling, False)
                cp.start()
                passed.append(cp)
        for a in range(n):
            for j in range(3):
                copy(a, 3 + j, chip_ids[j], 1 - c, (x, y, c), False).wait_recv()
        for j in range(3):
            conv_copy(j, chip_ids[j], (x, y, c)).wait_recv()
        for cp in first + passed:
            cp.wait_send()
        for cp in local:
            cp.wait()

    n_sem = 6 * n + 3
    return pl.pallas_call(
        body, name="weights_allgather", in_specs=[HBM] * (n + 1), out_specs=[HBM] * (n + 1),
        out_shape=[jax.ShapeDtypeStruct((NCHIP,) + s.shape, s.dtype) for s in shards]
        + [jax.ShapeDtypeStruct((NCHIP,) + conv.shape, conv.dtype)],
        scratch_shapes=[pltpu.SemaphoreType.DMA((n_sem,)), pltpu.SemaphoreType.DMA((n_sem,)),
                        pltpu.SemaphoreType.DMA((n + 1,))],
        compiler_params=_cparams(),
    )(*shards, conv)


def _grads_pair_exchange(grads):
    n = len(grads)

    def body(*refs):
        src = refs[:n]
        mine, theirs = refs[n:2 * n], refs[2 * n:3 * n]
        send_sems, recv_sems, local_sems = refs[3 * n:]
        x, y, c = _place()
        copies = []
        for a in range(n):
            rows = src[a].shape[1]
            cp = pltpu.make_async_copy(src[a].at[:, _half_rows(rows, c)], mine[a], local_sems.at[a])
            cp.start()
            copies.append(cp)
        for a in range(n):
            rows = src[a].shape[1]
            cp = pltpu.make_async_remote_copy(
                src_ref=src[a].at[:, _half_rows(rows, 1 - c)], dst_ref=theirs[a], send_sem=send_sems.at[a],
                recv_sem=recv_sems.at[a], device_id=(x, y, 1 - c), device_id_type=MESH)
            cp.start()
            copies.append(cp)
        for cp in copies:
            cp.wait()

    halves = [jax.ShapeDtypeStruct((g.shape[0], g.shape[1] // 2, g.shape[2]), g.dtype) for g in grads]
    res = pl.pallas_call(
        body, name="grads_pair_exchange", in_specs=[HBM] * n, out_specs=[HBM] * (2 * n), out_shape=halves + halves,
        scratch_shapes=[pltpu.SemaphoreType.DMA((n,)), pltpu.SemaphoreType.DMA((n,)), pltpu.SemaphoreType.DMA((n,))],
        compiler_params=_cparams(),
    )(*grads)
    return res[:n], res[n:]


def _pair_sum(mine, theirs):
    n = len(mine)

    def body(*refs):
        for a in range(n):
            refs[2 * n + a][...] = (refs[a][...].astype(F32) + refs[n + a][...].astype(F32)).astype(BF16)

    specs = [pl.BlockSpec((None,) + g.shape[1:], lambda j: (j, 0, 0)) for g in mine]
    return pl.pallas_call(
        body, name="grads_pair_sum", grid=(NCHIP,), in_specs=specs + specs, out_specs=specs,
        out_shape=[jax.ShapeDtypeStruct(g.shape, BF16) for g in mine], compiler_params=_cparams(),
    )(*mine, *theirs)


def _grads_chip_exchange(parts):
    n = len(parts)

    def body(*refs):
        src, outs = refs[:n], refs[n:2 * n]
        send_sems, recv_sems, local_sems = refs[2 * n:]
        x, y, c = _place()
        copies = []
        for a in range(n):
            cp = pltpu.make_async_copy(src[a].at[2 * x + y], outs[a].at[3], local_sems.at[a])
            cp.start()
            copies.append(cp)
        for a in range(n):
            for k, chip in enumerate(_other_chips(x, y)):
                cp = pltpu.make_async_remote_copy(
                    src_ref=src[a].at[2 * chip[0] + chip[1]], dst_ref=outs[a].at[k], send_sem=send_sems.at[3 * a + k],
                    recv_sem=recv_sems.at[3 * a + k], device_id=(*chip, c), device_id_type=MESH)
                cp.start()
                copies.append(cp)
        for cp in copies:
            cp.wait()

    return pl.pallas_call(
        body, name="grads_chip_exchange", in_specs=[HBM] * n, out_specs=[HBM] * n,
        out_shape=[jax.ShapeDtypeStruct(p.shape, p.dtype) for p in parts],
        scratch_shapes=[pltpu.SemaphoreType.DMA((3 * n,)), pltpu.SemaphoreType.DMA((3 * n,)), pltpu.SemaphoreType.DMA((n,))],
        compiler_params=_cparams(),
    )(*parts)


def _chip_sum(slots):
    n = len(slots)
    steps = 4

    def body(*refs):
        for a in range(n):
            src = refs[a]
            refs[n + a][...] = ((src[3].astype(F32) + src[0].astype(F32)) + src[1].astype(F32)) + src[2].astype(F32)

    in_specs = [pl.BlockSpec((NCHIP, g.shape[1] // steps, g.shape[2]), lambda i: (0, i, 0)) for g in slots]
    out_specs = [pl.BlockSpec((g.shape[1] // steps, g.shape[2]), lambda i: (i, 0)) for g in slots]
    return pl.pallas_call(
        body, name="grads_chip_sum", grid=(steps,), in_specs=in_specs, out_specs=out_specs,
        out_shape=[jax.ShapeDtypeStruct(g.shape[1:], F32) for g in slots], compiler_params=_cparams(),
    )(*slots)


def _grads_pair_share(halves):
    n = len(halves)

    def body(*refs):
        src, outs = refs[:n], refs[n:2 * n]
        send_sems, recv_sems, local_sems = refs[2 * n:]
        x, y, c = _place()
        copies = []
        for a in range(n):
            rows = _half_rows(2 * src[a].shape[0], c)
            cp = pltpu.make_async_copy(src[a], outs[a].at[rows], local_sems.at[a])
            cp.start()
            copies.append(cp)
            cp = pltpu.make_async_remote_copy(
                src_ref=src[a], dst_ref=outs[a].at[rows], send_sem=send_sems.at[a], recv_sem=recv_sems.at[a],
                device_id=(x, y, 1 - c), device_id_type=MESH)
            cp.start()
            copies.append(cp)
        for cp in copies:
            cp.wait()

    return pl.pallas_call(
        body, name="grads_pair_share", in_specs=[HBM] * n, out_specs=[HBM] * n,
        out_shape=[jax.ShapeDtypeStruct((2 * h.shape[0], h.shape[1]), F32) for h in halves],
        scratch_shapes=[pltpu.SemaphoreType.DMA((n,)), pltpu.SemaphoreType.DMA((n,)), pltpu.SemaphoreType.DMA((n,))],
        compiler_params=_cparams(),
    )(*halves)


def _adamw_math(w, g, m, v):
    nm = ADAM_B1 * m + (1.0 - ADAM_B1) * g
    nv = ADAM_B2 * v + (1.0 - ADAM_B2) * jnp.square(g)
    m_hat = nm / (1.0 - ADAM_B1 ** ADAM_STEP)
    v_hat = nv / (1.0 - ADAM_B2 ** ADAM_STEP)
    return -ADAM_LR * (m_hat / (jnp.sqrt(v_hat) + ADAM_EPS) + ADAM_WD * w), nm, nv


def _adamw_big(ws, gs, ms, vs):
    n = len(ws)
    steps = 8

    def body(*refs):
        for a in range(n):
            w = refs[a][...]
            g = refs[n + a][...][:, :w.shape[1]]
            d, nm, nv = _adamw_math(w, g, refs[2 * n + a][...], refs[3 * n + a][...])
            refs[4 * n + a][...] = g
            refs[5 * n + a][...] = d
            refs[6 * n + a][...] = nm
            refs[7 * n + a][...] = nv

    def spec(arr):
        return pl.BlockSpec((arr.shape[0] // steps, arr.shape[1]), lambda i: (i, 0))

    specs = [spec(w) for w in ws]
    shapes = [jax.ShapeDtypeStruct(w.shape, F32) for w in ws]
    res = pl.pallas_call(
        body, name="adamw_big", grid=(steps,), in_specs=specs + [spec(g) for g in gs] + specs * 2, out_specs=specs * 4,
        out_shape=shapes * 4, compiler_params=_cparams(),
    )(*ws, *gs, *ms, *vs)
    return res[:n], res[n:2 * n], res[2 * n:3 * n], res[3 * n:]


NORM_NAMES = ("pre_mix_norm", "post_mix_norm", "pre_mlp_norm", "post_mlp_norm")
SMALL_NAMES = NORM_NAMES + ("gdn_conv_w", "fox_f_bias", "gdn_dt_bias", "gdn_a_log", "fox_out_norm", "gdn_out_norm")
CONV_COLS = 3 * DGDN // NCHIP


def _small_update(d_norms, d_conv, sums, d_fox_norm, d_gdn_norm, ws, ms, vs):
    n = len(SMALL_NAMES)
    shapes = [w.shape for w in ws]

    def body(*refs):
        dn_ref, dconv_ref, sums_ref, dfn_ref, dgn_ref = refs[:5]
        w_refs, m_refs, v_refs = refs[5:5 + n], refs[5 + n:5 + 2 * n], refs[5 + 2 * n:5 + 3 * n]
        outs = refs[5 + 3 * n:5 + 7 * n]
        g_norms, g_conv, g_sums, g_fn, g_gn, send_sems, recv_sems, local_sem = refs[5 + 7 * n:]
        x, y, c = _place()
        me = 4 * x + 2 * y + c
        g_norms[me] = dn_ref[...]
        g_sums[me] = sums_ref[...]
        g_fn[me] = dfn_ref[...]
        g_gn[me] = dgn_ref[...]

        def conv_cols(chip_index):
            return dconv_ref.at[:, pl.ds(pl.multiple_of(chip_index * CONV_COLS, LANES), CONV_COLS)]

        own = pltpu.make_async_copy(conv_cols(2 * x + y), g_conv.at[me], local_sem)
        own.start()
        copies = []
        for k in range(1, NDEV):
            px, py, pc = x ^ ((k >> 2) & 1), y ^ ((k >> 1) & 1), c ^ (k & 1)
            pairs = [(dn_ref, g_norms), (conv_cols(2 * px + py), g_conv), (sums_ref, g_sums), (dfn_ref, g_fn), (dgn_ref, g_gn)]
            for a, (src, dst) in enumerate(pairs):
                cp = pltpu.make_async_remote_copy(
                    src_ref=src, dst_ref=dst.at[me], send_sem=send_sems.at[5 * (k - 1) + a],
                    recv_sem=recv_sems.at[5 * (k - 1) + a], device_id=(px, py, pc), device_id_type=MESH)
                cp.start()
                copies.append(cp)
        own.wait()
        for cp in copies:
            cp.wait()

        def total(buf):
            acc = buf[0]
            for i in range(1, NDEV):
                acc = acc + buf[i]
            return acc

        t_norms, t_conv, t_sums, t_fn, t_gn = total(g_norms), total(g_conv), total(g_sums), total(g_fn), total(g_gn)
        grads = [t_norms[i:i + 1, :] for i in range(4)] + [
            t_conv, t_sums[0:1, 0:NFH], t_sums[1:2, 0:NGH], t_sums[2:3, 0:NGH], t_fn[:, 0:FHD], t_gn]
        for a in range(n):
            d, nm, nv = _adamw_math(w_refs[a][...], grads[a], m_refs[a][...], v_refs[a][...])
            outs[a][...] = grads[a]
            outs[n + a][...] = d
            outs[2 * n + a][...] = nm
            outs[3 * n + a][...] = nv

    n_sem = 5 * (NDEV - 1)
    out_shape = [jax.ShapeDtypeStruct(s, F32) for s in shapes] * 4
    res = pl.pallas_call(
        body, name="small_update", in_specs=[VMEM] * (5 + 3 * n), out_specs=[VMEM] * (4 * n), out_shape=out_shape,
        scratch_shapes=[pltpu.VMEM((NDEV, 4, D), F32), pltpu.VMEM((NDEV, CONV_K, CONV_COLS), F32),
                        pltpu.VMEM((NDEV, 8, LANES), F32), pltpu.VMEM((NDEV, 1, LANES), F32),
                        pltpu.VMEM((NDEV, 1, LANES), F32), pltpu.SemaphoreType.DMA((n_sem,)),
                        pltpu.SemaphoreType.DMA((n_sem,)), pltpu.SemaphoreType.DMA],
        compiler_params=_cparams(),
    )(d_norms, d_conv, sums, d_fox_norm, d_gdn_norm, *ws, *ms, *vs)
    return res[:n], res[n:2 * n], res[2 * n:3 * n], res[3 * n:]


def _to_padded_cols(w):
    pieces = [w[:, part * DFOX + hp * LANES:part * DFOX + (hp + 1) * LANES] for hp in range(NPAIR) for part in range(3)]
    pieces += [w[:, 1544:3080], w[:, 3088:3600], w[:, 1536:1544], w[:, 3080:3088],
               jnp.zeros((w.shape[0], 2 * LANES - 16), w.dtype)]
    return jnp.concatenate(pieces, axis=1)


def _from_padded_cols(w):
    c0 = BLK_SMALL * LANES
    fox = [w[:, (3 * hp + part) * LANES:(3 * hp + part + 1) * LANES] for part in range(3) for hp in range(NPAIR)]
    return jnp.concatenate(fox + [w[:, c0:c0 + 8], w[:, BLK_GDN * LANES:BLK_GZ * LANES], w[:, c0 + 8:c0 + 16],
                                  w[:, BLK_GZ * LANES:BLK_SMALL * LANES]], axis=1)


def _local_step(x, target, win_p, wout, wup3, wdown, pre_mix_norm, fox_f_bias, fox_out_norm, conv_w, gdn_a_log,
                gdn_dt_bias, gdn_out_norm, post_mix_norm, pre_mlp_norm, post_mlp_norm):
    bias_vec = jnp.zeros((1, LANES), F32).at[0, 0:NFH].set(fox_f_bias).at[0, LANE_G:LANE_G + NGH].set(gdn_dt_bias)
    alog_vec = jnp.zeros((1, LANES), F32).at[0, LANE_G:LANE_G + NGH].set(gdn_a_log)
    w2 = jnp.concatenate([fox_out_norm, fox_out_norm], axis=1)

    h = _pre_norm(x, pre_mix_norm)
    proj = _matmul(h, win_p, tn=768, tk=1024, name="mm_proj")
    gates, gates_t = _gates(proj, bias_vec, alog_vec)
    mix, fox_o, lse = _fox_fwd(proj, gates, gates_t, w2)
    qkv = _gdn_pre(proj, conv_w)
    u, w, qd, kd, a_intra, gl, t_inv = _gdn_prep(qkv, gates)
    mix, gdn_raw, states = _gdn_scan(u, w, qd, kd, a_intra, gl, proj, gdn_out_norm, mix)
    mixed = _matmul(mix, wout, tk=1024, name="mm_out")
    x1, h2 = _post_mix(x, mixed, post_mix_norm, pre_mlp_norm)

    def relu2(acc):
        r = jnp.maximum(acc, 0.0)
        return acc, r * r

    up, act = _matmul(h2, wup3, b3=True, tn=1024, tk=1024, out_dtypes=(F32, BF16), epilogue=relu2, name="mm_up")
    y = _matmul(act, wdown, tk=1024, name="mm_down")
    dx2, dy, d_post_mlp, loss_row = _loss_head(x1, y, post_mlp_norm, target)

    dwdown = _matmul(act, dy, ta=True, tk=1024, out_dtypes=(BF16,), name="mm_dwdown")

    def relu2_bwd(acc, upv):
        return (acc * 2.0 * jnp.maximum(upv, 0.0),)

    dup = _matmul(dy, wdown, tb=True, tk=1024, out_dtypes=(BF16,), extra=(up,), epilogue=relu2_bwd, name="mm_dact")
    dwup3 = _matmul(h2, dup, ta=True, tn=1024, tk=1024, out_dtypes=(BF16,), o3=True, name="mm_dwup")
    dh2 = _matmul(dup, wup3, tb=True, b3=True, tk=1024, name="mm_dh2")
    dx1, dmixed, d_pre_mlp, d_post_mix = _mid_bwd(dh2, x1, pre_mlp_norm, dx2, mixed, post_mix_norm)
    dwout = _matmul(mix, dmixed, ta=True, tk=1024, out_dtypes=(BF16,), name="mm_dwout")
    dmix = _matmul(dmixed, wout, tb=True, tk=1024, name="mm_dmix")

    dfox, delta, d_fox_norm = _fox_norm_bwd(fox_o, dmix, w2)
    dproj, dcum_fox = _fox_bwd(proj, dfox, gates, gates_t, lse, delta)
    dproj, du, dw, dqd, dkd, da, dgl, d_gdn_norm = _gdn_scan_bwd(dmix, gdn_raw, proj, gdn_out_norm, u, w, qd, kd,
                                                                 a_intra, gl, states, dproj)
    dqkv, dgates_gdn = _gdn_prep_bwd(qkv, gates, t_inv, du, dw, dqd, dkd, da, dgl)
    dproj, d_conv = _gdn_pre_bwd(proj, conv_w, dqkv, dproj)
    dproj, sums = _gates_bwd(proj, bias_vec, alog_vec, dgates_gdn, dcum_fox, dproj)

    dwin_p = _matmul(h, dproj, ta=True, tn=768, tk=1024, out_dtypes=(BF16,), name="mm_dwin")
    dh = _matmul(dproj, win_p, tb=True, tk=768, name="mm_dh")
    grad_x, d_pre_mix = _pre_norm_bwd(dh, x, pre_mix_norm, dx1)

    d_norms = jnp.concatenate([d_pre_mix, d_post_mix, d_pre_mlp, d_post_mlp], axis=0)
    return loss_row[0, 0], grad_x, (dwin_p, dwout, dwup3, dwdown), (d_norms, d_conv, sums, d_fox_norm, d_gdn_norm)


def kernel(x, pre_mix_norm, w_in, fox_f_bias, fox_out_norm, gdn_conv_w, gdn_a_log, gdn_dt_bias, gdn_out_norm, w_out, post_mix_norm, pre_mlp_norm, w_up, w_down, post_mlp_norm, loss_target, m_pre_mix_norm, m_w_in, m_fox_f_bias, m_fox_out_norm, m_gdn_conv_w, m_gdn_a_log, m_gdn_dt_bias, m_gdn_out_norm, m_w_out, m_post_mix_norm, m_pre_mlp_norm, m_w_up, m_w_down, m_post_mlp_norm, v_pre_mix_norm, v_w_in, v_fox_f_bias, v_fox_out_norm, v_gdn_conv_w, v_gdn_a_log, v_gdn_dt_bias, v_gdn_out_norm, v_w_out, v_post_mix_norm, v_pre_mlp_norm, v_w_up, v_w_down, v_post_mlp_norm):
    weights = dict(pre_mix_norm=pre_mix_norm, w_in=w_in, fox_f_bias=fox_f_bias, fox_out_norm=fox_out_norm, gdn_conv_w=gdn_conv_w,
                   gdn_a_log=gdn_a_log, gdn_dt_bias=gdn_dt_bias, gdn_out_norm=gdn_out_norm, w_out=w_out, post_mix_norm=post_mix_norm,
                   pre_mlp_norm=pre_mlp_norm, w_up=w_up, w_down=w_down, post_mlp_norm=post_mlp_norm)
    m_in = dict(pre_mix_norm=m_pre_mix_norm, w_in=m_w_in, fox_f_bias=m_fox_f_bias, fox_out_norm=m_fox_out_norm, gdn_conv_w=m_gdn_conv_w,
                gdn_a_log=m_gdn_a_log, gdn_dt_bias=m_gdn_dt_bias, gdn_out_norm=m_gdn_out_norm, w_out=m_w_out, post_mix_norm=m_post_mix_norm,
                pre_mlp_norm=m_pre_mlp_norm, w_up=m_w_up, w_down=m_w_down, post_mlp_norm=m_post_mlp_norm)
    v_in = dict(pre_mix_norm=v_pre_mix_norm, w_in=v_w_in, fox_f_bias=v_fox_f_bias, fox_out_norm=v_fox_out_norm, gdn_conv_w=v_gdn_conv_w,
                gdn_a_log=v_gdn_a_log, gdn_dt_bias=v_gdn_dt_bias, gdn_out_norm=v_gdn_out_norm, w_out=v_w_out, post_mix_norm=v_post_mix_norm,
                pre_mlp_norm=v_pre_mlp_norm, w_up=v_w_up, w_down=v_w_down, post_mlp_norm=v_post_mlp_norm)
    order_w = ("pre_mix_norm", "w_in", "fox_f_bias", "fox_out_norm", "gdn_conv_w", "gdn_a_log", "gdn_dt_bias", "gdn_out_norm", "w_out",
               "post_mix_norm", "pre_mlp_norm", "w_up", "w_down", "post_mlp_norm")
    big = ("w_in", "w_out", "w_up", "w_down")

    def row(v):
        return v if v.ndim == 2 else v.reshape(1, -1)

    cw = DPROJ // NCHIP
    shards = [jnp.pad(w_in.astype(BF16), ((0, 0), (0, D - cw)))] + [weights[n].astype(BF16) for n in big[1:]]
    win_g, wout_g, wup3, wdown_g, conv_g = _weights_allgather(shards, gdn_conv_w)
    win_p = _to_padded_cols(win_g[:, :, :cw].transpose(1, 0, 2).reshape(D, DPROJ))
    conv_full = conv_g.transpose(1, 0, 2).reshape(CONV_K, 3 * DGDN)

    loss_local, grad_x, (dwin_p, dwout, dwup3, dwdown), small = _local_step(
        x[0], loss_target[0], win_p, wout_g.reshape(D, D), wup3, wdown_g.reshape(DFF, D), row(pre_mix_norm), fox_f_bias,
        row(fox_out_norm), conv_full, gdn_a_log, gdn_dt_bias, row(gdn_out_norm), row(post_mix_norm), row(pre_mlp_norm),
        row(post_mlp_norm))
    loss = lax.psum(loss_local, ("x", "y", "c"))

    dwin3 = jnp.pad(_from_padded_cols(dwin_p).reshape(D, NCHIP, cw).transpose(1, 0, 2), ((0, 0), (0, 0), (0, D - cw)))
    blocks = [dwin3, dwout.reshape(NCHIP, D // NCHIP, D), dwup3, dwdown.reshape(NCHIP, DFF // NCHIP, D)]
    mine, theirs = _grads_pair_exchange(blocks)
    slots = _grads_chip_exchange(_pair_sum(mine, theirs))
    g_shards = _grads_pair_share(_chip_sum(slots))

    g_small, d_small, nm_small, nv_small = _small_update(
        *small, [row(weights[n]) for n in SMALL_NAMES], [row(m_in[n]) for n in SMALL_NAMES], [row(v_in[n]) for n in SMALL_NAMES])

    g_big, d_big, nm_big, nv_big = _adamw_big([weights[n] for n in big], g_shards, [m_in[n] for n in big], [v_in[n] for n in big])

    grads, delta, new_m, new_v = {}, {}, {}, {}
    for i, n in enumerate(big):
        grads[n], delta[n], new_m[n], new_v[n] = g_big[i], d_big[i], nm_big[i], nv_big[i]
    for i, n in enumerate(SMALL_NAMES):
        shape = weights[n].shape
        grads[n], delta[n], new_m[n], new_v[n] = (g_small[i].reshape(shape), d_small[i].reshape(shape),
                                                  nm_small[i].reshape(shape), nv_small[i].reshape(shape))
    return (loss, grad_x[None], *[grads[n] for n in order_w], *[delta[n] for n in order_w], *[new_m[n] for n in order_w],
            *[new_v[n] for n in order_w])
```

```python
import jax
import jax.numpy as jnp
from jax import lax
from jax.experimental import pallas as pl
from jax.experimental.pallas import tpu as pltpu

F32 = jnp.float32
BF16 = jnp.bfloat16
MESH = pl.DeviceIdType.MESH

S = 2048
D = 1024
NFH, FHD = 8, 64
NPAIR = NFH // 2
NGH, GHD = 4, 128
DFOX = NFH * FHD
DGDN = NGH * GHD
CHUNK = 64
NCH = S // CHUNK
CONV_K = 4
DFF = 4 * D
EPS = 1e-6
DPROJ = 3600
LANES = 128
DPROJ_PAD = 3840
BLK_GDN = 12
BLK_GZ = 24
BLK_SMALL = 28
NCHIP = 4
NDEV = 8
VMEM_LIMIT = 56 * 1024 * 1024

ADAM_LR = 0.001
ADAM_B1 = 0.9
ADAM_B2 = 0.999
ADAM_EPS = 1e-08
ADAM_WD = 0.01
ADAM_STEP = 10


def _cparams(**kw):
    return pltpu.CompilerParams(vmem_limit_bytes=VMEM_LIMIT, **kw)


def _dn(ca, cb):
    return (((ca,), (cb,)), ((), ()))


def _dot(a, b, ca=1, cb=0):
    return lax.dot_general(a.astype(BF16), b.astype(BF16), _dn(ca, cb), preferred_element_type=F32)


def _hdot(a, b, ca=1, cb=0):
    return lax.dot_general(a.astype(F32), b.astype(F32), _dn(ca, cb), precision=lax.Precision.HIGHEST,
                           preferred_element_type=F32)


@jax.custom_vjp
def _mm_nn(a, b):
    return _dot(a, b, 1, 0)


def _mm_nn_fwd(a, b):
    return _dot(a, b, 1, 0), (a, b)


def _mm_nn_bwd(res, g):
    a, b = res
    return _dot(g, b, 1, 1), _dot(a, g, 0, 0)


_mm_nn.defvjp(_mm_nn_fwd, _mm_nn_bwd)


@jax.custom_vjp
def _mm_nt(a, b):
    return _dot(a, b, 1, 1)


def _mm_nt_fwd(a, b):
    return _dot(a, b, 1, 1), (a, b)


def _mm_nt_bwd(res, g):
    a, b = res
    return _dot(g, b, 1, 0), _dot(g, a, 0, 0)


_mm_nt.defvjp(_mm_nt_fwd, _mm_nt_bwd)


@jax.custom_vjp
def _saved_inverse(m, t_inv):
    del m
    return t_inv


def _saved_inverse_fwd(m, t_inv):
    del m
    return t_inv, t_inv


def _saved_inverse_bwd(t_inv, g):
    return -_hdot(_hdot(t_inv, g, 0, 0), t_inv, 1, 1), jnp.zeros_like(t_inv)


_saved_inverse.defvjp(_saved_inverse_fwd, _saved_inverse_bwd)


def _sigmoid(z):
    return 1.0 / (1.0 + jnp.exp(-z))


def _softplus(z):
    return jnp.maximum(z, 0.0) + jnp.log(1.0 + jnp.exp(-jnp.abs(z)))


def _silu(z):
    return z * _sigmoid(z)


def _rms_scale(x):
    return lax.rsqrt(jnp.mean(x * x, axis=-1, keepdims=True) + EPS)


def _rms_bwd(x, w, g):
    r = _rms_scale(x)
    gw = g * w
    dx = r * gw - x * (r * r * r) * jnp.mean(gw * x, axis=-1, keepdims=True)
    return dx, g * x * r


def _matmul(a, b, *, name, ta=False, tb=False, tm=512, tn=512, tk=512, out_dtypes=(F32,), b3=False, o3=False,
            extra=(), epilogue=None):
    m, k = (a.shape[1], a.shape[0]) if ta else a.shape
    if b3:
        n = b.shape[1] if tb else b.shape[0] * b.shape[2]
        kb = b.shape[0] * b.shape[2] if tb else b.shape[1]
    else:
        n, kb = (b.shape[0], b.shape[1]) if tb else (b.shape[1], b.shape[0])
    assert kb == k, (name, kb, k)
    tm, tn, tk = min(tm, m), min(tn, n), min(tk, k)
    assert m % tm == 0 and n % tn == 0 and k % tk == 0, (name, m, n, k, tm, tn, tk)
    nk = k // tk
    n_extra = len(extra)
    n_out = len(out_dtypes)

    def body(*refs):
        a_ref, b_ref = refs[0], refs[1]
        extra_refs = refs[2:2 + n_extra]
        out_refs = refs[2 + n_extra:2 + n_extra + n_out]
        acc_ref = refs[-1]
        kk = pl.program_id(2)

        @pl.when(kk == 0)
        def _():
            acc_ref[...] = jnp.zeros_like(acc_ref)

        acc_ref[...] += _dot(a_ref[...], b_ref[...], 0 if ta else 1, 1 if tb else 0)

        @pl.when(kk == nk - 1)
        def _():
            acc = acc_ref[...]
            outs = (acc,) if epilogue is None else epilogue(acc, *[r[...] for r in extra_refs])
            for o_ref, val in zip(out_refs, outs):
                o_ref[...] = val.astype(o_ref.dtype)

    a_spec = pl.BlockSpec((tk, tm), lambda i, j, kk: (kk, i)) if ta else pl.BlockSpec((tm, tk), lambda i, j, kk: (i, kk))
    if b3 and tb:
        assert b.shape[2] == tk
        b_spec = pl.BlockSpec((None, tn, tk), lambda i, j, kk: (kk, j, 0))
    elif b3:
        assert b.shape[2] == tn
        b_spec = pl.BlockSpec((None, tk, tn), lambda i, j, kk: (j, kk, 0))
    elif tb:
        b_spec = pl.BlockSpec((tn, tk), lambda i, j, kk: (j, kk))
    else:
        b_spec = pl.BlockSpec((tk, tn), lambda i, j, kk: (kk, j))
    tile = pl.BlockSpec((tm, tn), lambda i, j, kk: (i, j))
    out_specs = [tile] * n_out
    out_shape = [jax.ShapeDtypeStruct((m, n), dt) for dt in out_dtypes]
    if o3:
        out_specs[0] = pl.BlockSpec((None, tm, tn), lambda i, j, kk: (j, i, 0))
        out_shape[0] = jax.ShapeDtypeStruct((n // tn, m, tn), out_dtypes[0])
    res = pl.pallas_call(
        body, name=name, grid=(m // tm, n // tn, nk),
        in_specs=[a_spec, b_spec] + [tile] * n_extra, out_specs=out_specs, out_shape=out_shape,
        scratch_shapes=[pltpu.VMEM((tm, tn), F32)],
        compiler_params=_cparams(dimension_semantics=("parallel", "parallel", "arbitrary")),
    )(a, b, *extra)
    return res[0] if n_out == 1 else res


TR = 256


def _row_spec(cols):
    return pl.BlockSpec((TR, cols), lambda i: (i, 0))


def _vec_spec(cols):
    return pl.BlockSpec((1, cols), lambda i: (0, 0))


def _pre_norm(x, w):
    def body(x_ref, w_ref, h_ref):
        xv = x_ref[...]
        h_ref[...] = (xv * _rms_scale(xv) * w_ref[...]).astype(BF16)

    return pl.pallas_call(
        body, name="pre_norm", grid=(S // TR,), in_specs=[_row_spec(D), _vec_spec(D)], out_specs=_row_spec(D),
        out_shape=jax.ShapeDtypeStruct((S, D), BF16), compiler_params=_cparams(),
    )(x, w)


def _post_mix(x, mixed, w_post, w_pre_mlp):
    def body(x_ref, m_ref, wp_ref, wm_ref, x1_ref, h2_ref):
        mv = m_ref[...]
        x1 = x_ref[...] + mv * _rms_scale(mv) * wp_ref[...]
        x1_ref[...] = x1
        h2_ref[...] = (x1 * _rms_scale(x1) * wm_ref[...]).astype(BF16)

    return pl.pallas_call(
        body, name="post_mix", grid=(S // TR,),
        in_specs=[_row_spec(D), _row_spec(D), _vec_spec(D), _vec_spec(D)], out_specs=[_row_spec(D), _row_spec(D)],
        out_shape=[jax.ShapeDtypeStruct((S, D), F32), jax.ShapeDtypeStruct((S, D), BF16)], compiler_params=_cparams(),
    )(x, mixed, w_post, w_pre_mlp)


def _loss_head(x1, y, w_post_mlp, target):
    def body(x1_ref, y_ref, w_ref, t_ref, dx2_ref, dy_ref, dw_ref, loss_ref):
        i = pl.program_id(0)
        yv = y_ref[...]
        w = w_ref[...]
        x2 = x1_ref[...] + yv * _rms_scale(yv) * w
        err = x2 - t_ref[...]
        dx2 = err * (1.0 / D)
        dx2_ref[...] = dx2
        dy, dwt = _rms_bwd(yv, w, dx2)
        dy_ref[...] = dy.astype(BF16)

        @pl.when(i == 0)
        def _():
            dw_ref[...] = jnp.zeros_like(dw_ref)
            loss_ref[...] = jnp.zeros_like(loss_ref)

        dw_ref[...] += jnp.sum(dwt, axis=0, keepdims=True)
        part = 0.5 * jnp.sum(jnp.mean(err * err, axis=-1, keepdims=True), axis=0, keepdims=True)
        loss_ref[...] += jnp.broadcast_to(part, loss_ref.shape)

    return pl.pallas_call(
        body, name="loss_head", grid=(S // TR,),
        in_specs=[_row_spec(D), _row_spec(D), _vec_spec(D), _row_spec(D)],
        out_specs=[_row_spec(D), _row_spec(D), _vec_spec(D), _vec_spec(LANES)],
        out_shape=[jax.ShapeDtypeStruct((S, D), F32), jax.ShapeDtypeStruct((S, D), BF16),
                   jax.ShapeDtypeStruct((1, D), F32), jax.ShapeDtypeStruct((1, LANES), F32)],
        compiler_params=_cparams(),
    )(x1, y, w_post_mlp, target)


def _mid_bwd(dh2, x1, w_pre_mlp, dx2, mixed, w_post):
    def body(dh2_ref, x1_ref, wm_ref, dx2_ref, m_ref, wp_ref, dx1_ref, dm_ref, dwm_ref, dwp_ref):
        i = pl.program_id(0)
        dxa, dwm = _rms_bwd(x1_ref[...], wm_ref[...], dh2_ref[...])
        dx1 = dx2_ref[...] + dxa
        dx1_ref[...] = dx1
        dm, dwp = _rms_bwd(m_ref[...], wp_ref[...], dx1)
        dm_ref[...] = dm.astype(BF16)

        @pl.when(i == 0)
        def _():
            dwm_ref[...] = jnp.zeros_like(dwm_ref)
            dwp_ref[...] = jnp.zeros_like(dwp_ref)

        dwm_ref[...] += jnp.sum(dwm, axis=0, keepdims=True)
        dwp_ref[...] += jnp.sum(dwp, axis=0, keepdims=True)

    return pl.pallas_call(
        body, name="mid_bwd", grid=(S // TR,),
        in_specs=[_row_spec(D), _row_spec(D), _vec_spec(D), _row_spec(D), _row_spec(D), _vec_spec(D)],
        out_specs=[_row_spec(D), _row_spec(D), _vec_spec(D), _vec_spec(D)],
        out_shape=[jax.ShapeDtypeStruct((S, D), F32), jax.ShapeDtypeStruct((S, D), BF16),
                   jax.ShapeDtypeStruct((1, D), F32), jax.ShapeDtypeStruct((1, D), F32)],
        compiler_params=_cparams(),
    )(dh2, x1, w_pre_mlp, dx2, mixed, w_post)


def _pre_norm_bwd(dh, x, w, dx1):
    def body(dh_ref, x_ref, w_ref, dx1_ref, dx_ref, dw_ref):
        i = pl.program_id(0)
        dxa, dwt = _rms_bwd(x_ref[...], w_ref[...], dh_ref[...])
        dx_ref[...] = dx1_ref[...] + dxa

        @pl.when(i == 0)
        def _():
            dw_ref[...] = jnp.zeros_like(dw_ref)

        dw_ref[...] += jnp.sum(dwt, axis=0, keepdims=True)

    return pl.pallas_call(
        body, name="pre_norm_bwd", grid=(S // TR,),
        in_specs=[_row_spec(D), _row_spec(D), _vec_spec(D), _row_spec(D)], out_specs=[_row_spec(D), _vec_spec(D)],
        out_shape=[jax.ShapeDtypeStruct((S, D), F32), jax.ShapeDtypeStruct((1, D), F32)], compiler_params=_cparams(),
    )(dh, x, w, dx1)


BQ = 256
NQ = S // BQ
LANE_BETA, LANE_G = 8, 12


def _gate_lanes(shape):
    lane = lax.broadcasted_iota(jnp.int32, shape, 1)
    return lane < LANE_BETA, (lane >= LANE_BETA) & (lane < LANE_G), (lane >= LANE_G) & (lane < LANE_G + NGH)


def _gates(proj, bias_vec, alog_vec):
    def body(s_ref, b_ref, a_ref, o_ref, t_ref, carry_ref):
        i = pl.program_id(0)

        @pl.when(i == 0)
        def _():
            carry_ref[...] = jnp.zeros_like(carry_ref)

        z = s_ref[...] + b_ref[...]
        tail = jnp.log(1.0 + jnp.exp(-jnp.abs(z)))
        sp = jnp.maximum(z, 0.0) + tail
        lf = jnp.minimum(z, 0.0) - tail
        r = lax.broadcasted_iota(jnp.int32, (BQ, BQ), 0)
        c = lax.broadcasted_iota(jnp.int32, (BQ, BQ), 1)
        tri = (c <= r).astype(F32)
        cum = _hdot(tri, lf) + carry_ref[...]
        carry_ref[...] = cum[BQ - 1:BQ, :]
        is_fox, is_beta, is_g = _gate_lanes(z.shape)
        out = jnp.where(is_fox, cum, jnp.where(is_beta, _sigmoid(z), jnp.where(is_g, -jnp.exp(a_ref[...]) * sp, 0.0)))
        o_ref[...] = out
        t_ref[...] = out.T

    return pl.pallas_call(
        body, name="gates", grid=(NQ,),
        in_specs=[pl.BlockSpec((BQ, LANES), lambda i: (i, BLK_SMALL)), _vec_spec(LANES), _vec_spec(LANES)],
        out_specs=[pl.BlockSpec((BQ, LANES), lambda i: (i, 0)), pl.BlockSpec((None, LANES, BQ), lambda i: (i, 0, 0))],
        out_shape=[jax.ShapeDtypeStruct((S, LANES), F32), jax.ShapeDtypeStruct((NQ, LANES, BQ), F32)],
        scratch_shapes=[pltpu.VMEM((1, LANES), F32)], compiler_params=_cparams(),
    )(proj, bias_vec, alog_vec)


def _gates_bwd(proj, bias_vec, alog_vec, dgates_gdn, dcum_fox, dproj):
    def body(s_ref, b_ref, a_ref, dg_ref, dc_ref, dproj_in, dproj_ref, red_ref, carry_ref):
        del dproj_in
        i = pl.program_id(0)

        @pl.when(i == 0)
        def _():
            carry_ref[...] = jnp.zeros_like(carry_ref)
            red_ref[...] = jnp.zeros_like(red_ref)

        z = s_ref[...] + b_ref[...]
        dg = dg_ref[...] + dc_ref[...]
        r = lax.broadcasted_iota(jnp.int32, (BQ, BQ), 0)
        c = lax.broadcasted_iota(jnp.int32, (BQ, BQ), 1)
        upper = (c >= r).astype(F32)
        dlf = _hdot(upper, dg) + carry_ref[...]
        carry_ref[...] = dlf[0:1, :]
        sig = _sigmoid(z)
        g_scale = -jnp.exp(a_ref[...])
        is_fox, is_beta, is_g = _gate_lanes(z.shape)
        ds = jnp.where(is_fox, dlf * (1.0 - sig), jnp.where(is_beta, dg * sig * (1.0 - sig), jnp.where(is_g, dg * g_scale * sig, 0.0)))
        dproj_ref[:, 0:LANES] = ds.astype(BF16)
        dproj_ref[:, LANES:2 * LANES] = jnp.zeros((BQ, LANES), BF16)
        dalog = jnp.where(is_g, dg * g_scale * _softplus(z), 0.0)
        sums = jnp.sum(ds, axis=0, keepdims=True)
        red_ref[0:1, :] += jnp.where(is_fox[0:1], sums, 0.0)
        red_ref[1:2, :] += pltpu.roll(jnp.where(is_g[0:1], sums, 0.0), LANES - LANE_G, 1)
        red_ref[2:3, :] += pltpu.roll(jnp.sum(dalog, axis=0, keepdims=True), LANES - LANE_G, 1)

    blk = pl.BlockSpec((BQ, LANES), lambda i: (NQ - 1 - i, 0))
    return pl.pallas_call(
        body, name="gates_bwd", grid=(NQ,),
        in_specs=[pl.BlockSpec((BQ, LANES), lambda i: (NQ - 1 - i, BLK_SMALL)), _vec_spec(LANES), _vec_spec(LANES), blk, blk,
                  pl.BlockSpec(memory_space=pl.ANY)],
        out_specs=[pl.BlockSpec((BQ, 2 * LANES), lambda i: (NQ - 1 - i, BLK_SMALL // 2)), pl.BlockSpec((8, LANES), lambda i: (0, 0))],
        out_shape=[jax.ShapeDtypeStruct((S, DPROJ_PAD), BF16), jax.ShapeDtypeStruct((8, LANES), F32)],
        input_output_aliases={5: 0},
        scratch_shapes=[pltpu.VMEM((1, LANES), F32)], compiler_params=_cparams(),
    )(proj, bias_vec, alog_vec, dgates_gdn, dcum_fox, dproj)


FOX_SCALE = FHD ** -0.5


def _head_mask(e):
    lane = lax.broadcasted_iota(jnp.int32, (1, LANES), 1)
    return (lane >= e * FHD) & (lane < (e + 1) * FHD)


def _lane_col(vals, index):
    lane = lax.broadcasted_iota(jnp.int32, vals.shape, 1)
    return jnp.sum(jnp.where(lane == index, vals, 0.0), axis=1, keepdims=True)


def _sublane_row(vals, index):
    row = lax.broadcasted_iota(jnp.int32, vals.shape, 0)
    return jnp.sum(jnp.where(row == index, vals, 0.0), axis=0, keepdims=True)


def _pair_cols(c0, c1):
    lane = lax.broadcasted_iota(jnp.int32, (c0.shape[0], 2), 1)
    return jnp.where(lane == 0, c0, c1)


def _fox_logits(q, k, cum_q, cum_k, qi, kj):
    s = _dot(q, k, 1, 1) * FOX_SCALE + (cum_q - cum_k)
    row = qi * BQ + lax.broadcasted_iota(jnp.int32, (BQ, BQ), 0)
    col = kj * BQ + lax.broadcasted_iota(jnp.int32, (BQ, BQ), 1)
    return jnp.where(col <= row, s, -jnp.inf)


def _head_rms(o, masks):
    o2 = o * o
    r = [lax.rsqrt(jnp.sum(jnp.where(mk, o2, 0.0), axis=1, keepdims=True) * (1.0 / FHD) + EPS) for mk in masks]
    return jnp.where(masks[0], r[0], r[1])


def _fox_fwd(proj, gates, gates_t, w2):
    def body(q_ref, k_ref, v_ref, g_ref, ct_ref, w_ref, mix_ref, o_ref, lse_ref, kb_ref, vb_ref):
        hp, qi = pl.program_id(0), pl.program_id(1)

        @pl.when(qi == 0)
        def _():
            kb_ref[...] = k_ref[...].astype(BF16)
            vb_ref[...] = v_ref[...].astype(BF16)

        masks = [_head_mask(0), _head_mask(1)]
        qblk = q_ref[...]
        gt = g_ref[...]
        qs = [jnp.where(mk, qblk, 0.0).astype(BF16) for mk in masks]
        cqs = [_lane_col(gt, 2 * hp + e) for e in range(2)]

        def step(kj, carry):
            rows = pl.ds(pl.multiple_of(kj * BQ, BQ), BQ)
            kb = kb_ref[rows, :]
            vb = vb_ref[rows, :]
            ctk = ct_ref[kj]
            new = []
            for e in range(2):
                m, l, acc = carry[e]
                s = _fox_logits(qs[e], kb, cqs[e], _sublane_row(ctk, 2 * hp + e), qi, kj)
                m_new = jnp.maximum(m, jnp.max(s, axis=-1, keepdims=True))
                p = jnp.exp(s - m_new)
                alpha = jnp.exp(m - m_new)
                new.append((m_new, alpha * l + jnp.sum(p, axis=-1, keepdims=True), alpha * acc + _dot(p, vb)))
            return tuple(new)

        one = (jnp.full((BQ, 1), -jnp.inf, F32), jnp.zeros((BQ, 1), F32), jnp.zeros((BQ, LANES), F32))
        (m0, l0, a0), (m1, l1, a1) = lax.fori_loop(0, qi + 1, step, (one, one))
        o = jnp.where(masks[0], a0 / l0, a1 / l1)
        o_ref[...] = o
        mix_ref[...] = (o * _head_rms(o, masks) * w_ref[...]).astype(BF16)
        lse_ref[...] = _pair_cols(m0 + jnp.log(l0), m1 + jnp.log(l1))

    def col(part):
        return pl.BlockSpec((S, LANES), lambda hp, i: (0, 3 * hp + part))

    blk = pl.BlockSpec((BQ, LANES), lambda hp, i: (i, hp))
    return pl.pallas_call(
        body, name="fox_fwd", grid=(NPAIR, NQ),
        in_specs=[pl.BlockSpec((BQ, LANES), lambda hp, i: (i, 3 * hp)), col(1), col(2),
                  pl.BlockSpec((BQ, LANES), lambda hp, i: (i, 0)), pl.BlockSpec((NQ, 8, BQ), lambda hp, i: (0, 0, 0)),
                  pl.BlockSpec((1, LANES), lambda hp, i: (0, 0))],
        out_specs=[blk, blk, pl.BlockSpec((None, BQ, 2), lambda hp, i: (hp, i, 0))],
        out_shape=[jax.ShapeDtypeStruct((S, D), BF16), jax.ShapeDtypeStruct((S, DFOX), F32),
                   jax.ShapeDtypeStruct((NPAIR, S, 2), F32)],
        scratch_shapes=[pltpu.VMEM((S, LANES), BF16), pltpu.VMEM((S, LANES), BF16)],
        compiler_params=_cparams(),
    )(proj, proj, proj, gates, gates_t, w2)


def _fox_norm_bwd(o, dmix, w2):
    def body(o_ref, g_ref, w_ref, do_ref, dl_ref, dw_ref):
        hp, qi = pl.program_id(0), pl.program_id(1)
        masks = [_head_mask(0), _head_mask(1)]
        ov = o_ref[...]
        g = g_ref[...]
        r = _head_rms(ov, masks)
        gw = g * w_ref[...]
        gwo = gw * ov
        mean = [jnp.sum(jnp.where(mk, gwo, 0.0), axis=1, keepdims=True) * (1.0 / FHD) for mk in masks]
        do = r * gw - ov * (r * r * r) * jnp.where(masks[0], mean[0], mean[1])
        do_ref[...] = do.astype(BF16)
        doo = do * ov
        dl_ref[...] = _pair_cols(*[jnp.sum(jnp.where(mk, doo, 0.0), axis=1, keepdims=True) for mk in masks])

        @pl.when((hp == 0) & (qi == 0))
        def _():
            dw_ref[...] = jnp.zeros_like(dw_ref)

        dw_ref[...] += jnp.sum(g * ov * r, axis=0, keepdims=True)

        @pl.when((hp == NPAIR - 1) & (qi == NQ - 1))
        def _():
            dw = dw_ref[...]
            dw_ref[...] = dw + pltpu.roll(dw, FHD, 1)

    blk = pl.BlockSpec((BQ, LANES), lambda hp, i: (i, hp))
    vec = pl.BlockSpec((1, LANES), lambda hp, i: (0, 0))
    return pl.pallas_call(
        body, name="fox_norm_bwd", grid=(NPAIR, NQ), in_specs=[blk, blk, vec],
        out_specs=[blk, pl.BlockSpec((None, BQ, 2), lambda hp, i: (hp, i, 0)), vec],
        out_shape=[jax.ShapeDtypeStruct((S, DFOX), BF16), jax.ShapeDtypeStruct((NPAIR, S, 2), F32),
                   jax.ShapeDtypeStruct((1, LANES), F32)],
        compiler_params=_cparams(),
    )(o, dmix, w2)


def _fox_bwd(proj, do, gates, gates_t, lse, delta):
    def body(q_ref, k_ref, v_ref, do_ref, g_ref, ct_ref, lse_ref, dl_ref, dproj_ref, dc_ref, qb_ref, dq_ref):
        hp, kj = pl.program_id(0), pl.program_id(1)

        @pl.when(kj == 0)
        def _():
            qb_ref[...] = q_ref[...].astype(BF16)
            dq_ref[...] = jnp.zeros_like(dq_ref)

        @pl.when((hp == 0) & (kj == 0))
        def _():
            dc_ref[...] = jnp.zeros_like(dc_ref)

        masks = [_head_mask(0), _head_mask(1)]
        kb = k_ref[...].astype(BF16)
        vb = v_ref[...].astype(BF16)
        ctk = ct_ref[...]
        cks = [_sublane_row(ctk, 2 * hp + e) for e in range(2)]
        lane = lax.broadcasted_iota(jnp.int32, (BQ, LANES), 1)

        def step(qi, carry):
            dk, dv, cs = carry
            rows = pl.ds(pl.multiple_of(qi * BQ, BQ), BQ)
            qv = qb_ref[rows, :]
            dov = do_ref[rows, :]
            gt = g_ref[rows, :]
            lse2 = lse_ref[rows, :]
            dl2 = dl_ref[rows, :]
            dq = jnp.zeros((BQ, LANES), F32)
            dc = jnp.zeros((BQ, LANES), F32)
            cs_new = []
            for e in range(2):
                h = 2 * hp + e
                qe = jnp.where(masks[e], qv, jnp.zeros_like(qv))
                doe = jnp.where(masks[e], dov, jnp.zeros_like(dov))
                s = _fox_logits(qe, kb, _lane_col(gt, h), cks[e], qi, kj)
                p = jnp.exp(s - _lane_col(lse2, e))
                dp = _dot(doe, vb, 1, 1)
                ds = p * (dp - _lane_col(dl2, e))
                dv = dv + _dot(p, doe, 0, 0)
                dk = dk + _dot(ds, qe, 0, 0) * FOX_SCALE
                dq = dq + jnp.where(masks[e], _dot(ds, kb), 0.0) * FOX_SCALE
                cs_new.append(cs[e] + jnp.sum(ds, axis=0, keepdims=True))
                dc = dc + jnp.where(lane == h, jnp.sum(ds, axis=1, keepdims=True), 0.0)
            dq_ref[rows, :] += dq
            dc_ref[rows, :] += dc
            return dk, dv, tuple(cs_new)

        zero = jnp.zeros((BQ, LANES), F32)
        dk, dv, cs = lax.fori_loop(kj, NQ, step, (zero, zero, (jnp.zeros((1, BQ), F32), jnp.zeros((1, BQ), F32))))
        krows = pl.ds(pl.multiple_of(kj * BQ, BQ), BQ)
        dproj_ref[krows, LANES:2 * LANES] = dk.astype(BF16)
        dproj_ref[krows, 2 * LANES:3 * LANES] = dv.astype(BF16)
        r = lax.broadcasted_iota(jnp.int32, (BQ, BQ), 0)
        c = lax.broadcasted_iota(jnp.int32, (BQ, BQ), 1)
        dcol = jnp.zeros((BQ, LANES), F32)
        for e in range(2):
            col = jnp.sum(jnp.where(r == c, cs[e], 0.0), axis=1, keepdims=True)
            dcol = dcol + jnp.where(lane == 2 * hp + e, col, 0.0)
        dc_ref[krows, :] -= dcol

        @pl.when(kj == NQ - 1)
        def _():
            dproj_ref[:, 0:LANES] = dq_ref[...].astype(BF16)

    def blk(part):
        return pl.BlockSpec((BQ, LANES), lambda hp, j: (j, 3 * hp + part))

    pair = pl.BlockSpec((None, S, 2), lambda hp, j: (hp, 0, 0))
    return pl.pallas_call(
        body, name="fox_bwd", grid=(NPAIR, NQ),
        in_specs=[pl.BlockSpec((S, LANES), lambda hp, j: (0, 3 * hp)), blk(1), blk(2),
                  pl.BlockSpec((S, LANES), lambda hp, j: (0, hp)), pl.BlockSpec((S, LANES), lambda hp, j: (0, 0)),
                  pl.BlockSpec((None, 8, BQ), lambda hp, j: (j, 0, 0)), pair, pair],
        out_specs=[pl.BlockSpec((S, 3 * LANES), lambda hp, j: (0, hp)), pl.BlockSpec((S, LANES), lambda hp, j: (0, 0))],
        out_shape=[jax.ShapeDtypeStruct((S, DPROJ_PAD), BF16), jax.ShapeDtypeStruct((S, LANES), F32)],
        scratch_shapes=[pltpu.VMEM((S, LANES), BF16), pltpu.VMEM((S, LANES), F32)],
        compiler_params=_cparams(),
    )(proj, proj, proj, do, gates, gates_t, lse, delta)


NQKV = 3 * NGH
GDN_QSCALE = GHD ** -0.5


def _shift_down(x, s):
    if s == 0:
        return x
    row = lax.broadcasted_iota(jnp.int32, x.shape, 0)
    return jnp.where(row >= s, pltpu.roll(x, s, 0), 0.0)


def _shift_up(x, s):
    if s == 0:
        return x
    n = x.shape[0]
    row = lax.broadcasted_iota(jnp.int32, x.shape, 0)
    return jnp.where(row < n - s, pltpu.roll(x, n - s, 0), 0.0)


def _conv_pre(xv, wv):
    pre = xv * wv[CONV_K - 1:CONV_K, :]
    for j in range(CONV_K - 1):
        pre = pre + _shift_down(xv, CONV_K - 1 - j) * wv[j:j + 1, :]
    return pre


def _l2_factors(b):
    return b < 2 * NGH, jnp.where(b < NGH, GDN_QSCALE, 1.0)


def _gdn_pre(proj, conv_w):
    def body(x_ref, w_ref, o_ref):
        b = pl.program_id(0)
        c = _silu(_conv_pre(x_ref[...], w_ref[...]))
        normed, scale = _l2_factors(b)
        rs = lax.rsqrt(jnp.sum(c * c, axis=-1, keepdims=True) + EPS)
        o_ref[...] = c * jnp.where(normed, rs, 1.0) * scale

    return pl.pallas_call(
        body, name="gdn_pre", grid=(NQKV,),
        in_specs=[pl.BlockSpec((S, GHD), lambda b: (0, BLK_GDN + b)), pl.BlockSpec((CONV_K, GHD), lambda b: (0, b))],
        out_specs=pl.BlockSpec((S, GHD), lambda b: (0, b)),
        out_shape=jax.ShapeDtypeStruct((S, NQKV * GHD), F32), compiler_params=_cparams(),
    )(proj, conv_w)


def _gdn_pre_bwd(proj, conv_w, dqkv, dproj):
    def body(x_ref, w_ref, dy_ref, dproj_in, dx_ref, dw_ref):
        del dproj_in
        b = pl.program_id(0)
        xv = x_ref[...]
        wv = w_ref[...]
        pre = _conv_pre(xv, wv)
        sig = _sigmoid(pre)
        c = pre * sig
        normed, scale = _l2_factors(b)
        g = dy_ref[...] * scale
        rs = lax.rsqrt(jnp.sum(c * c, axis=-1, keepdims=True) + EPS)
        dc_n = rs * g - c * (rs * rs * rs) * jnp.sum(g * c, axis=-1, keepdims=True)
        dc = jnp.where(normed, dc_n, g)
        dpre = dc * sig * (1.0 + pre * (1.0 - sig))
        dx = dpre * wv[CONV_K - 1:CONV_K, :]
        for j in range(CONV_K - 1):
            dx = dx + _shift_up(dpre, CONV_K - 1 - j) * wv[j:j + 1, :]
        dx_ref[...] = dx.astype(BF16)
        for j in range(CONV_K):
            dw_ref[j:j + 1, :] = jnp.sum(dpre * _shift_down(xv, CONV_K - 1 - j), axis=0, keepdims=True)

    return pl.pallas_call(
        body, name="gdn_pre_bwd", grid=(NQKV,),
        in_specs=[pl.BlockSpec((S, GHD), lambda b: (0, BLK_GDN + b)), pl.BlockSpec((CONV_K, GHD), lambda b: (0, b)),
                  pl.BlockSpec((None, S, GHD), lambda b: (b // NGH, 0, b % NGH)), pl.BlockSpec(memory_space=pl.ANY)],
        out_specs=[pl.BlockSpec((S, GHD), lambda b: (0, BLK_GDN + b)), pl.BlockSpec((CONV_K, GHD), lambda b: (0, b))],
        out_shape=[jax.ShapeDtypeStruct((S, DPROJ_PAD), BF16), jax.ShapeDtypeStruct((CONV_K, NQKV * GHD), F32)],
        input_output_aliases={3: 0}, compiler_params=_cparams(),
    )(proj, conv_w, dqkv, dproj)


CB = 4
NCB = NCH // CB


def _chunk_prep(qs, ks, vs, gcols, bcols, t_saved=None):
    n = range(len(qs))
    r = lax.broadcasted_iota(jnp.int32, (CHUNK, CHUNK), 0)
    c = lax.broadcasted_iota(jnp.int32, (CHUNK, CHUNK), 1)
    incl = c <= r
    eye = (r == c).astype(F32)
    grow = [jnp.sum(gcols[i] * eye, axis=0, keepdims=True) for i in n]
    gc_col = [jnp.sum(jnp.where(incl, grow[i], 0.0), axis=1, keepdims=True) for i in n]
    gc_row = [jnp.sum(jnp.where(r <= c, gcols[i], 0.0), axis=0, keepdims=True) for i in n]
    decay = [jnp.exp(jnp.where(incl, gc_col[i] - gc_row[i], -jnp.inf)) for i in n]
    kb = [ks[i] * bcols[i] for i in n]
    vb = [vs[i] * bcols[i] for i in n]
    kk = [_mm_nt(kb[i], ks[i]) for i in n]
    m = [jnp.where(c < r, kk[i] * decay[i], 0.0) for i in n]
    if t_saved is None:
        t_inv = [eye - m[i] for i in n]
        p = [_hdot(m[i], m[i]) for i in n]
        for step in range(5):
            t_inv = [t_inv[i] + _hdot(t_inv[i], p[i]) for i in n]
            if step < 4:
                p = [_hdot(p[i], p[i]) for i in n]
    else:
        t_inv = [_saved_inverse(m[i], t_saved[i]) for i in n]
    egc = [jnp.exp(gc_col[i]) for i in n]
    u = [_mm_nn(t_inv[i], vb[i]) for i in n]
    w = [_mm_nn(t_inv[i], kb[i] * egc[i]) for i in n]
    qk = [_mm_nt(qs[i], ks[i]) for i in n]
    gc_last = [gc_col[i][CHUNK - 1:CHUNK, :] for i in n]
    return [(u[i], w[i], qk[i] * decay[i], qs[i] * egc[i], ks[i] * jnp.exp(gc_last[i] - gc_col[i]), jnp.exp(gc_last[i]),
             t_inv[i]) for i in n]


def _prep_specs():
    rows = CB * CHUNK
    qs = pl.BlockSpec((rows, GHD), lambda i, h: (i, h))
    ks = pl.BlockSpec((rows, GHD), lambda i, h: (i, NGH + h))
    vs = pl.BlockSpec((rows, GHD), lambda i, h: (i, 2 * NGH + h))
    gs = pl.BlockSpec((rows, LANES), lambda i, h: (i, 0))
    a_s = pl.BlockSpec((None, rows, CHUNK), lambda i, h: (h, i, 0))
    gl_s = pl.BlockSpec((None, CB, 1, LANES), lambda i, h: (h, i, 0, 0))
    return qs, ks, vs, gs, a_s, gl_s


def _gdn_prep(qkv, gates):
    def body(q_ref, k_ref, v_ref, g_ref, u_ref, w_ref, qd_ref, kd_ref, a_ref, gl_ref, t_ref):
        h = pl.program_id(1)
        chunks = [pl.ds(cidx * CHUNK, CHUNK) for cidx in range(CB)]
        gts = [g_ref[rows, :] for rows in chunks]
        outs = _chunk_prep([q_ref[rows, :] for rows in chunks], [k_ref[rows, :] for rows in chunks],
                           [v_ref[rows, :] for rows in chunks], [_lane_col(gt, LANE_G + h) for gt in gts],
                           [_lane_col(gt, LANE_BETA + h) for gt in gts])
        for cidx, rows in enumerate(chunks):
            u, w, a, qd, kd, gl, t_inv = outs[cidx]
            u_ref[rows, :] = u
            w_ref[rows, :] = w
            qd_ref[rows, :] = qd
            kd_ref[rows, :] = kd
            a_ref[rows, :] = a
            t_ref[rows, :] = t_inv
            gl_ref[cidx] = jnp.broadcast_to(gl, (1, LANES))

    qs, ks, vs, gs, a_s, gl_s = _prep_specs()
    tok = jax.ShapeDtypeStruct((S, DGDN), F32)
    sq = jax.ShapeDtypeStruct((NGH, S, CHUNK), F32)
    return pl.pallas_call(
        body, name="gdn_prep", grid=(NCB, NGH), in_specs=[qs, ks, vs, gs], out_specs=[qs, qs, qs, qs, a_s, gl_s, a_s],
        out_shape=[tok, tok, tok, tok, sq, jax.ShapeDtypeStruct((NGH, NCH, 1, LANES), F32), sq],
        compiler_params=_cparams(),
    )(qkv, qkv, qkv, gates)


def _gdn_prep_bwd(qkv, gates, t_inv, du, dw, dqd, dkd, da, dgl):
    def body(q_ref, k_ref, v_ref, g_ref, t_ref, du_ref, dw_ref, dqd_ref, dkd_ref, da_ref, dgl_ref, dqkv_ref, dg_ref):
        h = pl.program_id(1)

        @pl.when(h == 0)
        def _():
            dg_ref[...] = jnp.zeros_like(dg_ref)

        lane = lax.broadcasted_iota(jnp.int32, (CHUNK, LANES), 1)
        chunks = [pl.ds(cidx * CHUNK, CHUNK) for cidx in range(CB)]
        gts = [g_ref[rows, :] for rows in chunks]
        t_saved = [t_ref[rows, :] for rows in chunks]
        _, vjp = jax.vjp(lambda *args: [o[:6] for o in _chunk_prep(*args, t_saved=t_saved)],
                         [q_ref[rows, :] for rows in chunks], [k_ref[rows, :] for rows in chunks],
                         [v_ref[rows, :] for rows in chunks], [_lane_col(gt, LANE_G + h) for gt in gts],
                         [_lane_col(gt, LANE_BETA + h) for gt in gts])
        dqs, dks, dvs, dgcs, dbcs = vjp([(du_ref[rows, :], dw_ref[rows, :], da_ref[rows, :], dqd_ref[rows, :],
                                          dkd_ref[rows, :], dgl_ref[cidx][:, 0:1]) for cidx, rows in enumerate(chunks)])
        for cidx, rows in enumerate(chunks):
            dq, dk, dv, dgc, dbc = dqs[cidx], dks[cidx], dvs[cidx], dgcs[cidx], dbcs[cidx]
            dqkv_ref[0, rows, :] = dq
            dqkv_ref[1, rows, :] = dk
            dqkv_ref[2, rows, :] = dv
            dg_ref[rows, :] += jnp.where(lane == LANE_G + h, dgc, 0.0) + jnp.where(lane == LANE_BETA + h, dbc, 0.0)

    qs, ks, vs, gs, a_s, gl_s = _prep_specs()
    return pl.pallas_call(
        body, name="gdn_prep_bwd", grid=(NCB, NGH), in_specs=[qs, ks, vs, gs, a_s, qs, qs, qs, qs, a_s, gl_s],
        out_specs=[pl.BlockSpec((3, CB * CHUNK, GHD), lambda i, h: (0, i, h)), gs],
        out_shape=[jax.ShapeDtypeStruct((3, S, DGDN), F32), jax.ShapeDtypeStruct((S, LANES), F32)],
        compiler_params=_cparams(),
    )(qkv, qkv, qkv, gates, t_inv, du, dw, dqd, dkd, da, dgl)


def _scan_specs():
    hs = pl.BlockSpec((S, GHD), lambda h: (0, h))
    a_s = pl.BlockSpec((None, S, CHUNK), lambda h: (h, 0, 0))
    gl_s = pl.BlockSpec((None, NCH, 1, LANES), lambda h: (h, 0, 0, 0))
    st_s = pl.BlockSpec((None, NCH, GHD, GHD), lambda h: (h, 0, 0, 0))
    gz_s = pl.BlockSpec((S, GHD), lambda h: (0, BLK_GZ + h))
    mix_s = pl.BlockSpec((S, GHD), lambda h: (0, NPAIR + h))
    return hs, a_s, gl_s, st_s, gz_s, mix_s


def _gdn_scan(u, w, qd, kd, a, gl, proj, w_norm, mix):
    def body(u_ref, w_ref, qd_ref, kd_ref, a_ref, gl_ref, z_ref, wn_ref, mix_in, mix_ref, o_ref, st_ref):
        del mix_in

        def step(ci, state):
            rows = pl.ds(pl.multiple_of(ci * CHUNK, CHUNK), CHUNK)
            st_ref[ci] = state
            vn = u_ref[rows, :] - _dot(w_ref[rows, :], state)
            o_ref[rows, :] = _dot(qd_ref[rows, :], state) + _dot(a_ref[rows, :], vn)
            return state * gl_ref[ci] + _dot(kd_ref[rows, :], vn, 0, 0)

        lax.fori_loop(0, NCH, step, jnp.zeros((GHD, GHD), F32))
        ov = o_ref[...]
        mix_ref[...] = (ov * _rms_scale(ov) * wn_ref[...] * _silu(z_ref[...])).astype(BF16)

    hs, a_s, gl_s, st_s, gz_s, mix_s = _scan_specs()
    return pl.pallas_call(
        body, name="gdn_scan", grid=(NGH,),
        in_specs=[hs, hs, hs, hs, a_s, gl_s, gz_s, pl.BlockSpec((1, GHD), lambda h: (0, 0)), pl.BlockSpec(memory_space=pl.ANY)],
        out_specs=[mix_s, hs, st_s],
        out_shape=[jax.ShapeDtypeStruct((S, D), BF16), jax.ShapeDtypeStruct((S, DGDN), F32),
                   jax.ShapeDtypeStruct((NGH, NCH, GHD, GHD), F32)],
        input_output_aliases={8: 0}, compiler_params=_cparams(),
    )(u, w, qd, kd, a, gl, proj, w_norm, mix)


def _gdn_scan_bwd(dmix, o, proj, w_norm, u, w, qd, kd, a, gl, states, dproj):
    def body(dy_ref, o_ref, z_ref, wn_ref, u_ref, w_ref, qd_ref, kd_ref, a_ref, gl_ref, st_ref, dproj_in,
             dz_ref, du_ref, dw_ref, dqd_ref, dkd_ref, da_ref, dgl_ref, dwn_ref, do_ref):
        del dproj_in
        h = pl.program_id(0)
        ov = o_ref[...]
        zv = z_ref[...]
        wn = wn_ref[...]
        g = dy_ref[...]
        sig = _sigmoid(zv)
        dz_ref[...] = (g * (ov * _rms_scale(ov) * wn) * sig * (1.0 + zv * (1.0 - sig))).astype(BF16)
        do, dwt = _rms_bwd(ov, wn, g * zv * sig)
        do_ref[...] = do

        @pl.when(h == 0)
        def _():
            dwn_ref[...] = jnp.zeros_like(dwn_ref)

        dwn_ref[...] += jnp.sum(dwt, axis=0, keepdims=True)

        def step(t, dstate):
            ci = NCH - 1 - t
            rows = pl.ds(pl.multiple_of(ci * CHUNK, CHUNK), CHUNK)
            state = st_ref[ci]
            dov = do_ref[rows, :]
            wv = w_ref[rows, :]
            kdv = kd_ref[rows, :]
            vn = u_ref[rows, :] - _dot(wv, state)
            dvn = _dot(a_ref[rows, :], dov, 0, 0) + _dot(kdv, dstate)
            da_ref[rows, :] = _dot(dov, vn, 1, 1)
            dqd_ref[rows, :] = _dot(dov, state, 1, 1)
            dkd_ref[rows, :] = _dot(vn, dstate, 1, 1)
            dgl = jnp.sum(jnp.sum(dstate * state, axis=1, keepdims=True), axis=0, keepdims=True)
            dgl_ref[ci] = jnp.broadcast_to(dgl, (1, LANES))
            du_ref[rows, :] = dvn
            dw_ref[rows, :] = -_dot(dvn, state, 1, 1)
            return dstate * gl_ref[ci] + _dot(qd_ref[rows, :], dov, 0, 0) - _dot(wv, dvn, 0, 0)

        lax.fori_loop(0, NCH, step, jnp.zeros((GHD, GHD), F32))

    hs, a_s, gl_s, st_s, gz_s, mix_s = _scan_specs()
    vec = pl.BlockSpec((1, GHD), lambda h: (0, 0))
    tok = jax.ShapeDtypeStruct((S, DGDN), F32)
    return pl.pallas_call(
        body, name="gdn_scan_bwd", grid=(NGH,),
        in_specs=[mix_s, hs, gz_s, vec, hs, hs, hs, hs, a_s, gl_s, st_s, pl.BlockSpec(memory_space=pl.ANY)],
        out_specs=[gz_s, hs, hs, hs, hs, a_s, gl_s, vec],
        out_shape=[jax.ShapeDtypeStruct((S, DPROJ_PAD), BF16), tok, tok, tok, tok,
                   jax.ShapeDtypeStruct((NGH, S, CHUNK), F32), jax.ShapeDtypeStruct((NGH, NCH, 1, LANES), F32),
                   jax.ShapeDtypeStruct((1, GHD), F32)],
        input_output_aliases={11: 0}, scratch_shapes=[pltpu.VMEM((S, GHD), F32)], compiler_params=_cparams(),
    )(dmix, o, proj, w_norm, u, w, qd, kd, a, gl, states, dproj)


def _place():
    return lax.axis_index("x"), lax.axis_index("y"), lax.axis_index("c")


def _other_chips(x, y):
    return [(1 - x, y), (x, 1 - y), (1 - x, 1 - y)]


HBM = pl.BlockSpec(memory_space=pltpu.HBM)
VMEM = pl.BlockSpec(memory_space=pltpu.VMEM)


def _half_rows(ref_or_rows, half):
    rows = ref_or_rows // 2
    return pl.ds(pl.multiple_of(half * rows, rows), rows)


def _weights_allgather(shards, conv):
    n = len(shards)

    def body(*refs):
        src, conv_src = refs[:n], refs[n]
        outs, conv_out = refs[n + 1:2 * n + 1], refs[2 * n + 1]
        send_sems, recv_sems = refs[2 * n + 2:]
        x, y, c = _place()
        me_chip = 2 * x + y
        sibling = (x, y, 1 - c)
        chips = _other_chips(x, y)
        chip_ids = [2 * ch[0] + ch[1] for ch in chips]

        def copy(a, k, chip_index, half, to, from_src):
            rows = _half_rows(src[a].shape[0], half)
            dst = outs[a].at[chip_index, rows]
            return pltpu.make_async_remote_copy(
                src_ref=src[a].at[rows] if from_src else dst, dst_ref=dst, send_sem=send_sems.at[6 * a + k],
                recv_sem=recv_sems.at[6 * a + k], device_id=to, device_id_type=MESH)

        def conv_copy(k, chip_index, to):
            return pltpu.make_async_remote_copy(
                src_ref=conv_src, dst_ref=conv_out.at[chip_index], send_sem=send_sems.at[6 * n + k],
                recv_sem=recv_sems.at[6 * n + k], device_id=to, device_id_type=MESH)

        first = [copy(a, j, me_chip, c, (*chips[j], c), True) for a in range(n) for j in range(3)]
        first += [conv_copy(j, me_chip, (*chips[j], c)) for j in range(3)]
        for cp in first:
            cp.start()
        passed = []
        for a in range(n):
            for j in range(3):
                copy(a, j, chip_ids[j], c, (x, y, c), False).wait_recv()
                cp = copy(a, 3 + j, chip_ids[j], c, sibling, False)
                cp.start()
                passed.append(cp)
        for a in range(n):
            for j in range(3):
                copy(a, 3 + j, chip_ids[j], 1 - c, (x, y, c), False).wait_recv()
        for j in range(3):
            conv_copy(j, chip_ids[j], (x, y, c)).wait_recv()
        for cp in first + passed:
            cp.wait_send()

    n_sem = 6 * n + 3
    return pl.pallas_call(
        body, name="weights_allgather", in_specs=[HBM] * (n + 1), out_specs=[HBM] * (n + 1),
        out_shape=[jax.ShapeDtypeStruct((NCHIP,) + s.shape, s.dtype) for s in shards]
        + [jax.ShapeDtypeStruct((NCHIP,) + conv.shape, conv.dtype)],
        scratch_shapes=[pltpu.SemaphoreType.DMA((n_sem,)), pltpu.SemaphoreType.DMA((n_sem,))],
        compiler_params=_cparams(),
    )(*shards, conv)


def _with_own(gathered, own):
    x, y, _ = _place()
    return lax.dynamic_update_index_in_dim(gathered, own, 2 * x + y, axis=0)


def _grads_pair_exchange(grads):
    n = len(grads)

    def body(*refs):
        src, theirs = refs[:n], refs[n:2 * n]
        send_sems, recv_sems = refs[2 * n:]
        x, y, c = _place()
        copies = []
        for a in range(n):
            cp = pltpu.make_async_remote_copy(
                src_ref=src[a].at[:, _half_rows(src[a].shape[1], 1 - c)], dst_ref=theirs[a], send_sem=send_sems.at[a],
                recv_sem=recv_sems.at[a], device_id=(x, y, 1 - c), device_id_type=MESH)
            cp.start()
            copies.append(cp)
        for cp in copies:
            cp.wait()

    return pl.pallas_call(
        body, name="grads_pair_exchange", in_specs=[HBM] * n, out_specs=[HBM] * n,
        out_shape=[jax.ShapeDtypeStruct((g.shape[0], g.shape[1] // 2, g.shape[2]), g.dtype) for g in grads],
        scratch_shapes=[pltpu.SemaphoreType.DMA((n,)), pltpu.SemaphoreType.DMA((n,))],
        compiler_params=_cparams(),
    )(*grads)


def _pair_sum(grads, theirs):
    n = len(grads)

    def body(*refs):
        south = lax.axis_index("c") == 0
        for a in range(n):
            g = refs[a][...]
            half = g.shape[0] // 2
            mine = jnp.where(south, g[:half], g[half:])
            refs[2 * n + a][...] = (mine.astype(F32) + refs[n + a][...].astype(F32)).astype(BF16)

    def specs(arrs):
        return [pl.BlockSpec((None,) + g.shape[1:], lambda j: (j, 0, 0)) for g in arrs]

    return pl.pallas_call(
        body, name="grads_pair_sum", grid=(NCHIP,), in_specs=specs(grads) + specs(theirs), out_specs=specs(theirs),
        out_shape=[jax.ShapeDtypeStruct(g.shape, BF16) for g in theirs], compiler_params=_cparams(),
    )(*grads, *theirs)


def _grads_chip_exchange(parts):
    n = len(parts)

    def body(*refs):
        src, outs = refs[:n], refs[n:2 * n]
        send_sems, recv_sems = refs[2 * n:]
        x, y, c = _place()
        copies = []
        for a in range(n):
            for k, chip in enumerate(_other_chips(x, y)):
                cp = pltpu.make_async_remote_copy(
                    src_ref=src[a].at[2 * chip[0] + chip[1]], dst_ref=outs[a].at[k], send_sem=send_sems.at[3 * a + k],
                    recv_sem=recv_sems.at[3 * a + k], device_id=(*chip, c), device_id_type=MESH)
                cp.start()
                copies.append(cp)
        for cp in copies:
            cp.wait()

    return pl.pallas_call(
        body, name="grads_chip_exchange", in_specs=[HBM] * n, out_specs=[HBM] * n,
        out_shape=[jax.ShapeDtypeStruct((NCHIP - 1,) + p.shape[1:], p.dtype) for p in parts],
        scratch_shapes=[pltpu.SemaphoreType.DMA((3 * n,)), pltpu.SemaphoreType.DMA((3 * n,))],
        compiler_params=_cparams(),
    )(*parts)


def _chip_sum(parts, received):
    n = len(parts)
    steps = 4

    def body(*refs):
        chip = 2 * lax.axis_index("x") + lax.axis_index("y")
        for a in range(n):
            p, r = refs[a], refs[n + a]
            own = jnp.where(chip == 0, p[0], jnp.where(chip == 1, p[1], jnp.where(chip == 2, p[2], p[3])))
            refs[2 * n + a][...] = ((own.astype(F32) + r[0].astype(F32)) + r[1].astype(F32)) + r[2].astype(F32)

    def specs(arrs):
        return [pl.BlockSpec((g.shape[0], g.shape[1] // steps, g.shape[2]), lambda i: (0, i, 0)) for g in arrs]

    out_specs = [pl.BlockSpec((g.shape[1] // steps, g.shape[2]), lambda i: (i, 0)) for g in parts]
    return pl.pallas_call(
        body, name="grads_chip_sum", grid=(steps,), in_specs=specs(parts) + specs(received), out_specs=out_specs,
        out_shape=[jax.ShapeDtypeStruct(g.shape[1:], F32) for g in parts], compiler_params=_cparams(),
    )(*parts, *received)


def _grads_pair_share(halves):
    n = len(halves)

    def body(*refs):
        src, outs = refs[:n], refs[n:2 * n]
        send_sems, recv_sems = refs[2 * n:]
        x, y, c = _place()
        copies = []
        for a in range(n):
            cp = pltpu.make_async_remote_copy(
                src_ref=src[a], dst_ref=outs[a], send_sem=send_sems.at[a], recv_sem=recv_sems.at[a],
                device_id=(x, y, 1 - c), device_id_type=MESH)
            cp.start()
            copies.append(cp)
        for cp in copies:
            cp.wait()

    return pl.pallas_call(
        body, name="grads_pair_share", in_specs=[HBM] * n, out_specs=[HBM] * n,
        out_shape=[jax.ShapeDtypeStruct(h.shape, F32) for h in halves],
        scratch_shapes=[pltpu.SemaphoreType.DMA((n,)), pltpu.SemaphoreType.DMA((n,))],
        compiler_params=_cparams(),
    )(*halves)


def _adamw_math(w, g, m, v):
    nm = ADAM_B1 * m + (1.0 - ADAM_B1) * g
    nv = ADAM_B2 * v + (1.0 - ADAM_B2) * jnp.square(g)
    m_hat = nm / (1.0 - ADAM_B1 ** ADAM_STEP)
    v_hat = nv / (1.0 - ADAM_B2 ** ADAM_STEP)
    return -ADAM_LR * (m_hat / (jnp.sqrt(v_hat) + ADAM_EPS) + ADAM_WD * w), nm, nv


def _adamw_big(ws, g_mine, g_theirs, ms, vs):
    n = len(ws)
    steps = 8

    def body(*refs):
        own_half = (pl.program_id(0) // (steps // 2)) == lax.axis_index("c")
        for a in range(n):
            w = refs[a][...]
            g = jnp.where(own_half, refs[n + a][...], refs[2 * n + a][...])[:, :w.shape[1]]
            d, nm, nv = _adamw_math(w, g, refs[3 * n + a][...], refs[4 * n + a][...])
            refs[5 * n + a][...] = g
            refs[6 * n + a][...] = d
            refs[7 * n + a][...] = nm
            refs[8 * n + a][...] = nv

    specs = [pl.BlockSpec((w.shape[0] // steps, w.shape[1]), lambda i: (i, 0)) for w in ws]
    half_specs = [pl.BlockSpec((g.shape[0] // (steps // 2), g.shape[1]), lambda i: (i % (steps // 2), 0)) for g in g_mine]
    shapes = [jax.ShapeDtypeStruct(w.shape, F32) for w in ws]
    res = pl.pallas_call(
        body, name="adamw_big", grid=(steps,), in_specs=specs + half_specs * 2 + specs * 2, out_specs=specs * 4,
        out_shape=shapes * 4, compiler_params=_cparams(),
    )(*ws, *g_mine, *g_theirs, *ms, *vs)
    return res[:n], res[n:2 * n], res[2 * n:3 * n], res[3 * n:]


NORM_NAMES = ("pre_mix_norm", "post_mix_norm", "pre_mlp_norm", "post_mlp_norm")
SMALL_NAMES = NORM_NAMES + ("gdn_conv_w", "fox_f_bias", "gdn_dt_bias", "gdn_a_log", "fox_out_norm", "gdn_out_norm")
CONV_COLS = 3 * DGDN // NCHIP


def _small_update(d_norms, d_conv, sums, d_fox_norm, d_gdn_norm, ws, ms, vs):
    n = len(SMALL_NAMES)
    shapes = [w.shape for w in ws]

    def body(*refs):
        dn_ref, dconv_ref, sums_ref, dfn_ref, dgn_ref = refs[:5]
        w_refs, m_refs, v_refs = refs[5:5 + n], refs[5 + n:5 + 2 * n], refs[5 + 2 * n:5 + 3 * n]
        outs = refs[5 + 3 * n:5 + 7 * n]
        g_norms, g_conv, g_sums, g_fn, g_gn, send_sems, recv_sems, local_sem = refs[5 + 7 * n:]
        x, y, c = _place()
        me = 4 * x + 2 * y + c
        g_norms[me] = dn_ref[...]
        g_sums[me] = sums_ref[...]
        g_fn[me] = dfn_ref[...]
        g_gn[me] = dgn_ref[...]

        def conv_cols(chip_index):
            return dconv_ref.at[:, pl.ds(pl.multiple_of(chip_index * CONV_COLS, LANES), CONV_COLS)]

        own = pltpu.make_async_copy(conv_cols(2 * x + y), g_conv.at[me], local_sem)
        own.start()
        copies = []
        for k in range(1, NDEV):
            px, py, pc = x ^ ((k >> 2) & 1), y ^ ((k >> 1) & 1), c ^ (k & 1)
            pairs = [(dn_ref, g_norms), (conv_cols(2 * px + py), g_conv), (sums_ref, g_sums), (dfn_ref, g_fn), (dgn_ref, g_gn)]
            for a, (src, dst) in enumerate(pairs):
                cp = pltpu.make_async_remote_copy(
                    src_ref=src, dst_ref=dst.at[me], send_sem=send_sems.at[5 * (k - 1) + a],
                    recv_sem=recv_sems.at[5 * (k - 1) + a], device_id=(px, py, pc), device_id_type=MESH)
                cp.start()
                copies.append(cp)
        own.wait()
        for cp in copies:
            cp.wait()

        def total(buf):
            acc = buf[0]
            for i in range(1, NDEV):
                acc = acc + buf[i]
            return acc

        t_norms, t_conv, t_sums, t_fn, t_gn = total(g_norms), total(g_conv), total(g_sums), total(g_fn), total(g_gn)
        grads = [t_norms[i:i + 1, :] for i in range(4)] + [
            t_conv, t_sums[0:1, 0:NFH], t_sums[1:2, 0:NGH], t_sums[2:3, 0:NGH], t_fn[:, 0:FHD], t_gn]
        for a in range(n):
            d, nm, nv = _adamw_math(w_refs[a][...], grads[a], m_refs[a][...], v_refs[a][...])
            outs[a][...] = grads[a]
            outs[n + a][...] = d
            outs[2 * n + a][...] = nm
            outs[3 * n + a][...] = nv

    n_sem = 5 * (NDEV - 1)
    out_shape = [jax.ShapeDtypeStruct(s, F32) for s in shapes] * 4
    res = pl.pallas_call(
        body, name="small_update", in_specs=[VMEM] * (5 + 3 * n), out_specs=[VMEM] * (4 * n), out_shape=out_shape,
        scratch_shapes=[pltpu.VMEM((NDEV, 4, D), F32), pltpu.VMEM((NDEV, CONV_K, CONV_COLS), F32),
                        pltpu.VMEM((NDEV, 8, LANES), F32), pltpu.VMEM((NDEV, 1, LANES), F32),
                        pltpu.VMEM((NDEV, 1, LANES), F32), pltpu.SemaphoreType.DMA((n_sem,)),
                        pltpu.SemaphoreType.DMA((n_sem,)), pltpu.SemaphoreType.DMA],
        compiler_params=_cparams(),
    )(d_norms, d_conv, sums, d_fox_norm, d_gdn_norm, *ws, *ms, *vs)
    return res[:n], res[n:2 * n], res[2 * n:3 * n], res[3 * n:]


def _to_padded_cols(w):
    pieces = [w[:, part * DFOX + hp * LANES:part * DFOX + (hp + 1) * LANES] for hp in range(NPAIR) for part in range(3)]
    pieces += [w[:, 1544:3080], w[:, 3088:3600], w[:, 1536:1544], w[:, 3080:3088],
               jnp.zeros((w.shape[0], 2 * LANES - 16), w.dtype)]
    return jnp.concatenate(pieces, axis=1)


def _from_padded_cols(w):
    c0 = BLK_SMALL * LANES
    fox = [w[:, (3 * hp + part) * LANES:(3 * hp + part + 1) * LANES] for part in range(3) for hp in range(NPAIR)]
    return jnp.concatenate(fox + [w[:, c0:c0 + 8], w[:, BLK_GDN * LANES:BLK_GZ * LANES], w[:, c0 + 8:c0 + 16],
                                  w[:, BLK_GZ * LANES:BLK_SMALL * LANES]], axis=1)


def _local_step(x, target, win_p, wout, wup3, wdown, pre_mix_norm, fox_f_bias, fox_out_norm, conv_w, gdn_a_log,
                gdn_dt_bias, gdn_out_norm, post_mix_norm, pre_mlp_norm, post_mlp_norm):
    bias_vec = jnp.zeros((1, LANES), F32).at[0, 0:NFH].set(fox_f_bias).at[0, LANE_G:LANE_G + NGH].set(gdn_dt_bias)
    alog_vec = jnp.zeros((1, LANES), F32).at[0, LANE_G:LANE_G + NGH].set(gdn_a_log)
    w2 = jnp.concatenate([fox_out_norm, fox_out_norm], axis=1)

    h = _pre_norm(x, pre_mix_norm)
    proj = _matmul(h, win_p, tn=768, tk=1024, name="mm_proj")
    gates, gates_t = _gates(proj, bias_vec, alog_vec)
    mix, fox_o, lse = _fox_fwd(proj, gates, gates_t, w2)
    qkv = _gdn_pre(proj, conv_w)
    u, w, qd, kd, a_intra, gl, t_inv = _gdn_prep(qkv, gates)
    mix, gdn_raw, states = _gdn_scan(u, w, qd, kd, a_intra, gl, proj, gdn_out_norm, mix)
    mixed = _matmul(mix, wout, tk=1024, name="mm_out")
    x1, h2 = _post_mix(x, mixed, post_mix_norm, pre_mlp_norm)

    def relu2(acc):
        r = jnp.maximum(acc, 0.0)
        return acc, r * r

    up, act = _matmul(h2, wup3, b3=True, tn=1024, tk=1024, out_dtypes=(F32, BF16), epilogue=relu2, name="mm_up")
    y = _matmul(act, wdown, tk=1024, name="mm_down")
    dx2, dy, d_post_mlp, loss_row = _loss_head(x1, y, post_mlp_norm, target)

    dwdown = _matmul(act, dy, ta=True, tk=1024, out_dtypes=(BF16,), name="mm_dwdown")

    def relu2_bwd(acc, upv):
        return (acc * 2.0 * jnp.maximum(upv, 0.0),)

    dup = _matmul(dy, wdown, tb=True, tk=1024, out_dtypes=(BF16,), extra=(up,), epilogue=relu2_bwd, name="mm_dact")
    dwup3 = _matmul(h2, dup, ta=True, tn=1024, tk=1024, out_dtypes=(BF16,), o3=True, name="mm_dwup")
    dh2 = _matmul(dup, wup3, tb=True, b3=True, tk=1024, name="mm_dh2")
    dx1, dmixed, d_pre_mlp, d_post_mix = _mid_bwd(dh2, x1, pre_mlp_norm, dx2, mixed, post_mix_norm)
    dwout = _matmul(mix, dmixed, ta=True, tk=1024, out_dtypes=(BF16,), name="mm_dwout")
    dmix = _matmul(dmixed, wout, tb=True, tk=1024, name="mm_dmix")

    dfox, delta, d_fox_norm = _fox_norm_bwd(fox_o, dmix, w2)
    dproj, dcum_fox = _fox_bwd(proj, dfox, gates, gates_t, lse, delta)
    dproj, du, dw, dqd, dkd, da, dgl, d_gdn_norm = _gdn_scan_bwd(dmix, gdn_raw, proj, gdn_out_norm, u, w, qd, kd,
                                                                 a_intra, gl, states, dproj)
    dqkv, dgates_gdn = _gdn_prep_bwd(qkv, gates, t_inv, du, dw, dqd, dkd, da, dgl)
    dproj, d_conv = _gdn_pre_bwd(proj, conv_w, dqkv, dproj)
    dproj, sums = _gates_bwd(proj, bias_vec, alog_vec, dgates_gdn, dcum_fox, dproj)

    dwin_p = _matmul(h, dproj, ta=True, tn=768, tk=1024, out_dtypes=(BF16,), name="mm_dwin")
    dh = _matmul(dproj, win_p, tb=True, tk=768, name="mm_dh")
    grad_x, d_pre_mix = _pre_norm_bwd(dh, x, pre_mix_norm, dx1)

    d_norms = jnp.concatenate([d_pre_mix, d_post_mix, d_pre_mlp, d_post_mlp], axis=0)
    return loss_row[0, 0], grad_x, (dwin_p, dwout, dwup3, dwdown), (d_norms, d_conv, sums, d_fox_norm, d_gdn_norm)


def kernel(x, pre_mix_norm, w_in, fox_f_bias, fox_out_norm, gdn_conv_w, gdn_a_log, gdn_dt_bias, gdn_out_norm, w_out, post_mix_norm, pre_mlp_norm, w_up, w_down, post_mlp_norm, loss_target, m_pre_mix_norm, m_w_in, m_fox_f_bias, m_fox_out_norm, m_gdn_conv_w, m_gdn_a_log, m_gdn_dt_bias, m_gdn_out_norm, m_w_out, m_post_mix_norm, m_pre_mlp_norm, m_w_up, m_w_down, m_post_mlp_norm, v_pre_mix_norm, v_w_in, v_fox_f_bias, v_fox_out_norm, v_gdn_conv_w, v_gdn_a_log, v_gdn_dt_bias, v_gdn_out_norm, v_w_out, v_post_mix_norm, v_pre_mlp_norm, v_w_up, v_w_down, v_post_mlp_norm):
    weights = dict(pre_mix_norm=pre_mix_norm, w_in=w_in, fox_f_bias=fox_f_bias, fox_out_norm=fox_out_norm, gdn_conv_w=gdn_conv_w,
                   gdn_a_log=gdn_a_log, gdn_dt_bias=gdn_dt_bias, gdn_out_norm=gdn_out_norm, w_out=w_out, post_mix_norm=post_mix_norm,
                   pre_mlp_norm=pre_mlp_norm, w_up=w_up, w_down=w_down, post_mlp_norm=post_mlp_norm)
    m_in = dict(pre_mix_norm=m_pre_mix_norm, w_in=m_w_in, fox_f_bias=m_fox_f_bias, fox_out_norm=m_fox_out_norm, gdn_conv_w=m_gdn_conv_w,
                gdn_a_log=m_gdn_a_log, gdn_dt_bias=m_gdn_dt_bias, gdn_out_norm=m_gdn_out_norm, w_out=m_w_out, post_mix_norm=m_post_mix_norm,
                pre_mlp_norm=m_pre_mlp_norm, w_up=m_w_up, w_down=m_w_down, post_mlp_norm=m_post_mlp_norm)
    v_in = dict(pre_mix_norm=v_pre_mix_norm, w_in=v_w_in, fox_f_bias=v_fox_f_bias, fox_out_norm=v_fox_out_norm, gdn_conv_w=v_gdn_conv_w,
                gdn_a_log=v_gdn_a_log, gdn_dt_bias=v_gdn_dt_bias, gdn_out_norm=v_gdn_out_norm, w_out=v_w_out, post_mix_norm=v_post_mix_norm,
                pre_mlp_norm=v_pre_mlp_norm, w_up=v_w_up, w_down=v_w_down, post_mlp_norm=v_post_mlp_norm)
    order_w = ("pre_mix_norm", "w_in", "fox_f_bias", "fox_out_norm", "gdn_conv_w", "gdn_a_log", "gdn_dt_bias", "gdn_out_norm", "w_out",
               "post_mix_norm", "pre_mlp_norm", "w_up", "w_down", "post_mlp_norm")
    big = ("w_in", "w_out", "w_up", "w_down")

    def row(v):
        return v if v.ndim == 2 else v.reshape(1, -1)

    cw = DPROJ // NCHIP
    shards = [jnp.pad(w_in.astype(BF16), ((0, 0), (0, D - cw)))] + [weights[n].astype(BF16) for n in big[1:]]
    gathered = _weights_allgather(shards, gdn_conv_w)
    win_g, wout_g, wup3, wdown_g, conv_g = [_with_own(g, own) for g, own in zip(gathered, shards + [gdn_conv_w])]
    win_p = _to_padded_cols(win_g[:, :, :cw].transpose(1, 0, 2).reshape(D, DPROJ))
    conv_full = conv_g.transpose(1, 0, 2).reshape(CONV_K, 3 * DGDN)

    loss_local, grad_x, (dwin_p, dwout, dwup3, dwdown), small = _local_step(
        x[0], loss_target[0], win_p, wout_g.reshape(D, D), wup3, wdown_g.reshape(DFF, D), row(pre_mix_norm), fox_f_bias,
        row(fox_out_norm), conv_full, gdn_a_log, gdn_dt_bias, row(gdn_out_norm), row(post_mix_norm), row(pre_mlp_norm),
        row(post_mlp_norm))
    loss = lax.psum(loss_local, ("x", "y", "c"))

    dwin3 = jnp.pad(_from_padded_cols(dwin_p).reshape(D, NCHIP, cw).transpose(1, 0, 2), ((0, 0), (0, 0), (0, D - cw)))
    blocks = [dwin3, dwout.reshape(NCHIP, D // NCHIP, D), dwup3, dwdown.reshape(NCHIP, DFF // NCHIP, D)]
    pair_sums = _pair_sum(blocks, _grads_pair_exchange(blocks))
    g_mine = _chip_sum(pair_sums, _grads_chip_exchange(pair_sums))
    g_theirs = _grads_pair_share(g_mine)

    g_small, d_small, nm_small, nv_small = _small_update(
        *small, [row(weights[n]) for n in SMALL_NAMES], [row(m_in[n]) for n in SMALL_NAMES], [row(v_in[n]) for n in SMALL_NAMES])

    g_big, d_big, nm_big, nv_big = _adamw_big([weights[n] for n in big], g_mine, g_theirs, [m_in[n] for n in big],
                                              [v_in[n] for n in big])

    grads, delta, new_m, new_v = {}, {}, {}, {}
    for i, n in enumerate(big):
        grads[n], delta[n], new_m[n], new_v[n] = g_big[i], d_big[i], nm_big[i], nv_big[i]
    for i, n in enumerate(SMALL_NAMES):
        shape = weights[n].shape
        grads[n], delta[n], new_m[n], new_v[n] = (g_small[i].reshape(shape), d_small[i].reshape(shape),
                                                  nm_small[i].reshape(shape), nv_small[i].reshape(shape))
    return (loss, grad_x[None], *[grads[n] for n in order_w], *[delta[n] for n in order_w], *[new_m[n] for n in order_w],
            *[new_v[n] for n in order_w])
```

```python
import jax
import jax.numpy as jnp
from jax import lax
from jax.experimental import pallas as pl
from jax.experimental.pallas import tpu as pltpu

F32 = jnp.float32
BF16 = jnp.bfloat16
MESH = pl.DeviceIdType.MESH

S = 2048
D = 1024
NFH, FHD = 8, 64
NPAIR = NFH // 2
NGH, GHD = 4, 128
DFOX = NFH * FHD
DGDN = NGH * GHD
CHUNK = 64
NCH = S // CHUNK
CONV_K = 4
DFF = 4 * D
EPS = 1e-6
DPROJ = 3600
LANES = 128
DPROJ_PAD = 3840
BLK_GDN = 12
BLK_GZ = 24
BLK_SMALL = 28
NCHIP = 4
NDEV = 8
VMEM_LIMIT = 56 * 1024 * 1024

ADAM_LR = 0.001
ADAM_B1 = 0.9
ADAM_B2 = 0.999
ADAM_EPS = 1e-08
ADAM_WD = 0.01
ADAM_STEP = 10


def _cparams(**kw):
    return pltpu.CompilerParams(vmem_limit_bytes=VMEM_LIMIT, **kw)


def _dn(ca, cb):
    return (((ca,), (cb,)), ((), ()))


def _dot(a, b, ca=1, cb=0):
    return lax.dot_general(a.astype(BF16), b.astype(BF16), _dn(ca, cb), preferred_element_type=F32)


def _hdot(a, b, ca=1, cb=0):
    return lax.dot_general(a.astype(F32), b.astype(F32), _dn(ca, cb), precision=lax.Precision.HIGHEST,
                           preferred_element_type=F32)


@jax.custom_vjp
def _mm_nn(a, b):
    return _dot(a, b, 1, 0)


def _mm_nn_fwd(a, b):
    return _dot(a, b, 1, 0), (a, b)


def _mm_nn_bwd(res, g):
    a, b = res
    return _dot(g, b, 1, 1), _dot(a, g, 0, 0)


_mm_nn.defvjp(_mm_nn_fwd, _mm_nn_bwd)


@jax.custom_vjp
def _mm_nt(a, b):
    return _dot(a, b, 1, 1)


def _mm_nt_fwd(a, b):
    return _dot(a, b, 1, 1), (a, b)


def _mm_nt_bwd(res, g):
    a, b = res
    return _dot(g, b, 1, 0), _dot(g, a, 0, 0)


_mm_nt.defvjp(_mm_nt_fwd, _mm_nt_bwd)


@jax.custom_vjp
def _saved_inverse(m, t_inv):
    del m
    return t_inv


def _saved_inverse_fwd(m, t_inv):
    del m
    return t_inv, t_inv


def _saved_inverse_bwd(t_inv, g):
    return -_hdot(_hdot(t_inv, g, 0, 0), t_inv, 1, 1), jnp.zeros_like(t_inv)


_saved_inverse.defvjp(_saved_inverse_fwd, _saved_inverse_bwd)


def _sigmoid(z):
    return 1.0 / (1.0 + jnp.exp(-z))


def _softplus(z):
    return jnp.maximum(z, 0.0) + jnp.log(1.0 + jnp.exp(-jnp.abs(z)))


def _silu(z):
    return z * _sigmoid(z)


def _rms_scale(x):
    return lax.rsqrt(jnp.mean(x * x, axis=-1, keepdims=True) + EPS)


def _rms_bwd(x, w, g):
    r = _rms_scale(x)
    gw = g * w
    dx = r * gw - x * (r * r * r) * jnp.mean(gw * x, axis=-1, keepdims=True)
    return dx, g * x * r


def _matmul(a, b, *, name, ta=False, tb=False, tm=512, tn=512, tk=512, out_dtypes=(F32,), b3=False, o3=False,
            extra=(), epilogue=None):
    m, k = (a.shape[1], a.shape[0]) if ta else a.shape
    if b3:
        n = b.shape[1] if tb else b.shape[0] * b.shape[2]
        kb = b.shape[0] * b.shape[2] if tb else b.shape[1]
    else:
        n, kb = (b.shape[0], b.shape[1]) if tb else (b.shape[1], b.shape[0])
    assert kb == k, (name, kb, k)
    tm, tn, tk = min(tm, m), min(tn, n), min(tk, k)
    assert m % tm == 0 and n % tn == 0 and k % tk == 0, (name, m, n, k, tm, tn, tk)
    nk = k // tk
    n_extra = len(extra)
    n_out = len(out_dtypes)

    def body(*refs):
        a_ref, b_ref = refs[0], refs[1]
        extra_refs = refs[2:2 + n_extra]
        out_refs = refs[2 + n_extra:2 + n_extra + n_out]

        def finish(acc):
            outs = (acc,) if epilogue is None else epilogue(acc, *[r[...] for r in extra_refs])
            for o_ref, val in zip(out_refs, outs):
                o_ref[...] = val.astype(o_ref.dtype)

        part = _dot(a_ref[...], b_ref[...], 0 if ta else 1, 1 if tb else 0)
        if nk == 1:
            finish(part)
            return
        acc_ref = refs[-1]
        kk = pl.program_id(2)

        @pl.when(kk == 0)
        def _():
            acc_ref[...] = part

        @pl.when(kk > 0)
        def _():
            acc_ref[...] += part

        @pl.when(kk == nk - 1)
        def _():
            finish(acc_ref[...])

    a_spec = pl.BlockSpec((tk, tm), lambda i, j, kk: (kk, i)) if ta else pl.BlockSpec((tm, tk), lambda i, j, kk: (i, kk))
    if b3 and tb:
        assert b.shape[2] == tk
        b_spec = pl.BlockSpec((None, tn, tk), lambda i, j, kk: (kk, j, 0))
    elif b3:
        assert b.shape[2] == tn
        b_spec = pl.BlockSpec((None, tk, tn), lambda i, j, kk: (j, kk, 0))
    elif tb:
        b_spec = pl.BlockSpec((tn, tk), lambda i, j, kk: (j, kk))
    else:
        b_spec = pl.BlockSpec((tk, tn), lambda i, j, kk: (kk, j))
    tile = pl.BlockSpec((tm, tn), lambda i, j, kk: (i, j))
    out_specs = [tile] * n_out
    out_shape = [jax.ShapeDtypeStruct((m, n), dt) for dt in out_dtypes]
    if o3:
        out_specs[0] = pl.BlockSpec((None, tm, tn), lambda i, j, kk: (j, i, 0))
        out_shape[0] = jax.ShapeDtypeStruct((n // tn, m, tn), out_dtypes[0])
    res = pl.pallas_call(
        body, name=name, grid=(m // tm, n // tn, nk),
        in_specs=[a_spec, b_spec] + [tile] * n_extra, out_specs=out_specs, out_shape=out_shape,
        scratch_shapes=[pltpu.VMEM((tm, tn), F32)] if nk > 1 else [],
        compiler_params=_cparams(dimension_semantics=("parallel", "parallel", "arbitrary")),
    )(a, b, *extra)
    return res[0] if n_out == 1 else res


TR = 256


def _row_spec(cols):
    return pl.BlockSpec((TR, cols), lambda i: (i, 0))


def _vec_spec(cols):
    return pl.BlockSpec((1, cols), lambda i: (0, 0))


def _pre_norm(x, w):
    def body(x_ref, w_ref, h_ref):
        xv = x_ref[...]
        h_ref[...] = (xv * _rms_scale(xv) * w_ref[...]).astype(BF16)

    return pl.pallas_call(
        body, name="pre_norm", grid=(S // TR,), in_specs=[_row_spec(D), _vec_spec(D)], out_specs=_row_spec(D),
        out_shape=jax.ShapeDtypeStruct((S, D), BF16), compiler_params=_cparams(),
    )(x, w)


def _post_mix(x, mixed, w_post, w_pre_mlp):
    def body(x_ref, m_ref, wp_ref, wm_ref, x1_ref, h2_ref):
        mv = m_ref[...]
        x1 = x_ref[...] + mv * _rms_scale(mv) * wp_ref[...]
        x1_ref[...] = x1
        h2_ref[...] = (x1 * _rms_scale(x1) * wm_ref[...]).astype(BF16)

    return pl.pallas_call(
        body, name="post_mix", grid=(S // TR,),
        in_specs=[_row_spec(D), _row_spec(D), _vec_spec(D), _vec_spec(D)], out_specs=[_row_spec(D), _row_spec(D)],
        out_shape=[jax.ShapeDtypeStruct((S, D), F32), jax.ShapeDtypeStruct((S, D), BF16)], compiler_params=_cparams(),
    )(x, mixed, w_post, w_pre_mlp)


def _loss_head(x1, y, w_post_mlp, target):
    def body(x1_ref, y_ref, w_ref, t_ref, dx2_ref, dy_ref, dw_ref, loss_ref):
        i = pl.program_id(0)
        yv = y_ref[...]
        w = w_ref[...]
        x2 = x1_ref[...] + yv * _rms_scale(yv) * w
        err = x2 - t_ref[...]
        dx2 = err * (1.0 / D)
        dx2_ref[...] = dx2
        dy, dwt = _rms_bwd(yv, w, dx2)
        dy_ref[...] = dy.astype(BF16)

        @pl.when(i == 0)
        def _():
            dw_ref[...] = jnp.zeros_like(dw_ref)
            loss_ref[...] = jnp.zeros_like(loss_ref)

        dw_ref[...] += jnp.sum(dwt, axis=0, keepdims=True)
        part = 0.5 * jnp.sum(jnp.mean(err * err, axis=-1, keepdims=True), axis=0, keepdims=True)
        loss_ref[...] += jnp.broadcast_to(part, loss_ref.shape)

    return pl.pallas_call(
        body, name="loss_head", grid=(S // TR,),
        in_specs=[_row_spec(D), _row_spec(D), _vec_spec(D), _row_spec(D)],
        out_specs=[_row_spec(D), _row_spec(D), _vec_spec(D), _vec_spec(LANES)],
        out_shape=[jax.ShapeDtypeStruct((S, D), F32), jax.ShapeDtypeStruct((S, D), BF16),
                   jax.ShapeDtypeStruct((1, D), F32), jax.ShapeDtypeStruct((1, LANES), F32)],
        compiler_params=_cparams(),
    )(x1, y, w_post_mlp, target)


def _mid_bwd(dh2, x1, w_pre_mlp, dx2, mixed, w_post):
    def body(dh2_ref, x1_ref, wm_ref, dx2_ref, m_ref, wp_ref, dx1_ref, dm_ref, dwm_ref, dwp_ref):
        i = pl.program_id(0)
        dxa, dwm = _rms_bwd(x1_ref[...], wm_ref[...], dh2_ref[...])
        dx1 = dx2_ref[...] + dxa
        dx1_ref[...] = dx1
        dm, dwp = _rms_bwd(m_ref[...], wp_ref[...], dx1)
        dm_ref[...] = dm.astype(BF16)

        @pl.when(i == 0)
        def _():
            dwm_ref[...] = jnp.zeros_like(dwm_ref)
            dwp_ref[...] = jnp.zeros_like(dwp_ref)

        dwm_ref[...] += jnp.sum(dwm, axis=0, keepdims=True)
        dwp_ref[...] += jnp.sum(dwp, axis=0, keepdims=True)

    return pl.pallas_call(
        body, name="mid_bwd", grid=(S // TR,),
        in_specs=[_row_spec(D), _row_spec(D), _vec_spec(D), _row_spec(D), _row_spec(D), _vec_spec(D)],
        out_specs=[_row_spec(D), _row_spec(D), _vec_spec(D), _vec_spec(D)],
        out_shape=[jax.ShapeDtypeStruct((S, D), F32), jax.ShapeDtypeStruct((S, D), BF16),
                   jax.ShapeDtypeStruct((1, D), F32), jax.ShapeDtypeStruct((1, D), F32)],
        compiler_params=_cparams(),
    )(dh2, x1, w_pre_mlp, dx2, mixed, w_post)


def _pre_norm_bwd(dh, x, w, dx1):
    def body(dh_ref, x_ref, w_ref, dx1_ref, dx_ref, dw_ref):
        i = pl.program_id(0)
        dxa, dwt = _rms_bwd(x_ref[...], w_ref[...], dh_ref[...])
        dx_ref[...] = dx1_ref[...] + dxa

        @pl.when(i == 0)
        def _():
            dw_ref[...] = jnp.zeros_like(dw_ref)

        dw_ref[...] += jnp.sum(dwt, axis=0, keepdims=True)

    return pl.pallas_call(
        body, name="pre_norm_bwd", grid=(S // TR,),
        in_specs=[_row_spec(D), _row_spec(D), _vec_spec(D), _row_spec(D)], out_specs=[_row_spec(D), _vec_spec(D)],
        out_shape=[jax.ShapeDtypeStruct((S, D), F32), jax.ShapeDtypeStruct((1, D), F32)], compiler_params=_cparams(),
    )(dh, x, w, dx1)


BQ = 256
NQ = S // BQ
LANE_BETA, LANE_G = 8, 12


def _gate_lanes(shape):
    lane = lax.broadcasted_iota(jnp.int32, shape, 1)
    return lane < LANE_BETA, (lane >= LANE_BETA) & (lane < LANE_G), (lane >= LANE_G) & (lane < LANE_G + NGH)


def _gates(proj, bias_vec, alog_vec):
    def body(s_ref, b_ref, a_ref, o_ref, t_ref, carry_ref):
        i = pl.program_id(0)

        @pl.when(i == 0)
        def _():
            carry_ref[...] = jnp.zeros_like(carry_ref)

        z = s_ref[...] + b_ref[...]
        tail = jnp.log(1.0 + jnp.exp(-jnp.abs(z)))
        sp = jnp.maximum(z, 0.0) + tail
        lf = jnp.minimum(z, 0.0) - tail
        r = lax.broadcasted_iota(jnp.int32, (BQ, BQ), 0)
        c = lax.broadcasted_iota(jnp.int32, (BQ, BQ), 1)
        tri = (c <= r).astype(F32)
        cum = _hdot(tri, lf) + carry_ref[...]
        carry_ref[...] = cum[BQ - 1:BQ, :]
        is_fox, is_beta, is_g = _gate_lanes(z.shape)
        out = jnp.where(is_fox, cum, jnp.where(is_beta, _sigmoid(z), jnp.where(is_g, -jnp.exp(a_ref[...]) * sp, 0.0)))
        o_ref[...] = out
        t_ref[...] = out.T

    return pl.pallas_call(
        body, name="gates", grid=(NQ,),
        in_specs=[pl.BlockSpec((BQ, LANES), lambda i: (i, BLK_SMALL)), _vec_spec(LANES), _vec_spec(LANES)],
        out_specs=[pl.BlockSpec((BQ, LANES), lambda i: (i, 0)), pl.BlockSpec((None, LANES, BQ), lambda i: (i, 0, 0))],
        out_shape=[jax.ShapeDtypeStruct((S, LANES), F32), jax.ShapeDtypeStruct((NQ, LANES, BQ), F32)],
        scratch_shapes=[pltpu.VMEM((1, LANES), F32)], compiler_params=_cparams(),
    )(proj, bias_vec, alog_vec)


def _gates_bwd(proj, bias_vec, alog_vec, dgates_gdn, dcum_fox, dproj):
    def body(s_ref, b_ref, a_ref, dg_ref, dc_ref, dproj_in, dproj_ref, red_ref, carry_ref):
        del dproj_in
        i = pl.program_id(0)

        @pl.when(i == 0)
        def _():
            carry_ref[...] = jnp.zeros_like(carry_ref)
            red_ref[...] = jnp.zeros_like(red_ref)

        z = s_ref[...] + b_ref[...]
        dg = dg_ref[...] + dc_ref[...]
        r = lax.broadcasted_iota(jnp.int32, (BQ, BQ), 0)
        c = lax.broadcasted_iota(jnp.int32, (BQ, BQ), 1)
        upper = (c >= r).astype(F32)
        dlf = _hdot(upper, dg) + carry_ref[...]
        carry_ref[...] = dlf[0:1, :]
        sig = _sigmoid(z)
        g_scale = -jnp.exp(a_ref[...])
        is_fox, is_beta, is_g = _gate_lanes(z.shape)
        ds = jnp.where(is_fox, dlf * (1.0 - sig), jnp.where(is_beta, dg * sig * (1.0 - sig), jnp.where(is_g, dg * g_scale * sig, 0.0)))
        dproj_ref[:, 0:LANES] = ds.astype(BF16)
        dproj_ref[:, LANES:2 * LANES] = jnp.zeros((BQ, LANES), BF16)
        dalog = jnp.where(is_g, dg * g_scale * _softplus(z), 0.0)
        sums = jnp.sum(ds, axis=0, keepdims=True)
        red_ref[0:1, :] += jnp.where(is_fox[0:1], sums, 0.0)
        red_ref[1:2, :] += pltpu.roll(jnp.where(is_g[0:1], sums, 0.0), LANES - LANE_G, 1)
        red_ref[2:3, :] += pltpu.roll(jnp.sum(dalog, axis=0, keepdims=True), LANES - LANE_G, 1)

    blk = pl.BlockSpec((BQ, LANES), lambda i: (NQ - 1 - i, 0))
    return pl.pallas_call(
        body, name="gates_bwd", grid=(NQ,),
        in_specs=[pl.BlockSpec((BQ, LANES), lambda i: (NQ - 1 - i, BLK_SMALL)), _vec_spec(LANES), _vec_spec(LANES), blk, blk,
                  pl.BlockSpec(memory_space=pl.ANY)],
        out_specs=[pl.BlockSpec((BQ, 2 * LANES), lambda i: (NQ - 1 - i, BLK_SMALL // 2)), pl.BlockSpec((8, LANES), lambda i: (0, 0))],
        out_shape=[jax.ShapeDtypeStruct((S, DPROJ_PAD), BF16), jax.ShapeDtypeStruct((8, LANES), F32)],
        input_output_aliases={5: 0},
        scratch_shapes=[pltpu.VMEM((1, LANES), F32)], compiler_params=_cparams(),
    )(proj, bias_vec, alog_vec, dgates_gdn, dcum_fox, dproj)


FOX_SCALE = FHD ** -0.5


def _head_mask(e):
    lane = lax.broadcasted_iota(jnp.int32, (1, LANES), 1)
    return (lane >= e * FHD) & (lane < (e + 1) * FHD)


def _lane_col(vals, index):
    lane = lax.broadcasted_iota(jnp.int32, vals.shape, 1)
    return jnp.sum(jnp.where(lane == index, vals, 0.0), axis=1, keepdims=True)


def _sublane_row(vals, index):
    row = lax.broadcasted_iota(jnp.int32, vals.shape, 0)
    return jnp.sum(jnp.where(row == index, vals, 0.0), axis=0, keepdims=True)


def _pair_cols(c0, c1):
    lane = lax.broadcasted_iota(jnp.int32, (c0.shape[0], 2), 1)
    return jnp.where(lane == 0, c0, c1)


def _fox_logits(q, k, cum_q, cum_k, qi, kj):
    s = _dot(q, k, 1, 1) * FOX_SCALE + (cum_q - cum_k)
    row = qi * BQ + lax.broadcasted_iota(jnp.int32, (BQ, BQ), 0)
    col = kj * BQ + lax.broadcasted_iota(jnp.int32, (BQ, BQ), 1)
    return jnp.where(col <= row, s, -jnp.inf)


def _head_rms(o, masks):
    o2 = o * o
    r = [lax.rsqrt(jnp.sum(jnp.where(mk, o2, 0.0), axis=1, keepdims=True) * (1.0 / FHD) + EPS) for mk in masks]
    return jnp.where(masks[0], r[0], r[1])


def _fox_fwd(proj, gates, gates_t, w2):
    def body(q_ref, k_ref, v_ref, g_ref, ct_ref, w_ref, mix_ref, o_ref, lse_ref, kb_ref, vb_ref):
        hp, qi = pl.program_id(0), pl.program_id(1)

        @pl.when(qi == 0)
        def _():
            kb_ref[...] = k_ref[...].astype(BF16)
            vb_ref[...] = v_ref[...].astype(BF16)

        masks = [_head_mask(0), _head_mask(1)]
        qblk = q_ref[...]
        gt = g_ref[...]
        qs = [jnp.where(mk, qblk, 0.0).astype(BF16) for mk in masks]
        cqs = [_lane_col(gt, 2 * hp + e) for e in range(2)]

        def step(kj, carry):
            rows = pl.ds(pl.multiple_of(kj * BQ, BQ), BQ)
            kb = kb_ref[rows, :]
            vb = vb_ref[rows, :]
            ctk = ct_ref[kj]
            new = []
            for e in range(2):
                m, l, acc = carry[e]
                s = _fox_logits(qs[e], kb, cqs[e], _sublane_row(ctk, 2 * hp + e), qi, kj)
                m_new = jnp.maximum(m, jnp.max(s, axis=-1, keepdims=True))
                p = jnp.exp(s - m_new)
                alpha = jnp.exp(m - m_new)
                new.append((m_new, alpha * l + jnp.sum(p, axis=-1, keepdims=True), alpha * acc + _dot(p, vb)))
            return tuple(new)

        one = (jnp.full((BQ, 1), -jnp.inf, F32), jnp.zeros((BQ, 1), F32), jnp.zeros((BQ, LANES), F32))
        (m0, l0, a0), (m1, l1, a1) = lax.fori_loop(0, qi + 1, step, (one, one))
        o = jnp.where(masks[0], a0 / l0, a1 / l1)
        o_ref[...] = o
        mix_ref[...] = (o * _head_rms(o, masks) * w_ref[...]).astype(BF16)
        lse_ref[...] = _pair_cols(m0 + jnp.log(l0), m1 + jnp.log(l1))

    def col(part):
        return pl.BlockSpec((S, LANES), lambda hp, i: (0, 3 * hp + part))

    blk = pl.BlockSpec((BQ, LANES), lambda hp, i: (i, hp))
    return pl.pallas_call(
        body, name="fox_fwd", grid=(NPAIR, NQ),
        in_specs=[pl.BlockSpec((BQ, LANES), lambda hp, i: (i, 3 * hp)), col(1), col(2),
                  pl.BlockSpec((BQ, LANES), lambda hp, i: (i, 0)), pl.BlockSpec((NQ, 8, BQ), lambda hp, i: (0, 0, 0)),
                  pl.BlockSpec((1, LANES), lambda hp, i: (0, 0))],
        out_specs=[blk, blk, pl.BlockSpec((None, BQ, 2), lambda hp, i: (hp, i, 0))],
        out_shape=[jax.ShapeDtypeStruct((S, D), BF16), jax.ShapeDtypeStruct((S, DFOX), F32),
                   jax.ShapeDtypeStruct((NPAIR, S, 2), F32)],
        scratch_shapes=[pltpu.VMEM((S, LANES), BF16), pltpu.VMEM((S, LANES), BF16)],
        compiler_params=_cparams(),
    )(proj, proj, proj, gates, gates_t, w2)


def _fox_norm_bwd(o, dmix, w2):
    def body(o_ref, g_ref, w_ref, do_ref, dl_ref, dw_ref):
        hp, qi = pl.program_id(0), pl.program_id(1)
        masks = [_head_mask(0), _head_mask(1)]
        ov = o_ref[...]
        g = g_ref[...]
        r = _head_rms(ov, masks)
        gw = g * w_ref[...]
        gwo = gw * ov
        mean = [jnp.sum(jnp.where(mk, gwo, 0.0), axis=1, keepdims=True) * (1.0 / FHD) for mk in masks]
        do = r * gw - ov * (r * r * r) * jnp.where(masks[0], mean[0], mean[1])
        do_ref[...] = do.astype(BF16)
        doo = do * ov
        dl_ref[...] = _pair_cols(*[jnp.sum(jnp.where(mk, doo, 0.0), axis=1, keepdims=True) for mk in masks])

        @pl.when((hp == 0) & (qi == 0))
        def _():
            dw_ref[...] = jnp.zeros_like(dw_ref)

        dw_ref[...] += jnp.sum(g * ov * r, axis=0, keepdims=True)

        @pl.when((hp == NPAIR - 1) & (qi == NQ - 1))
        def _():
            dw = dw_ref[...]
            dw_ref[...] = dw + pltpu.roll(dw, FHD, 1)

    blk = pl.BlockSpec((BQ, LANES), lambda hp, i: (i, hp))
    vec = pl.BlockSpec((1, LANES), lambda hp, i: (0, 0))
    return pl.pallas_call(
        body, name="fox_norm_bwd", grid=(NPAIR, NQ), in_specs=[blk, blk, vec],
        out_specs=[blk, pl.BlockSpec((None, BQ, 2), lambda hp, i: (hp, i, 0)), vec],
        out_shape=[jax.ShapeDtypeStruct((S, DFOX), BF16), jax.ShapeDtypeStruct((NPAIR, S, 2), F32),
                   jax.ShapeDtypeStruct((1, LANES), F32)],
        compiler_params=_cparams(),
    )(o, dmix, w2)


def _fox_bwd(proj, do, gates, gates_t, lse, delta):
    def body(q_ref, k_ref, v_ref, do_ref, g_ref, ct_ref, lse_ref, dl_ref, dproj_ref, dc_ref, qb_ref, dq_ref):
        hp, kj = pl.program_id(0), pl.program_id(1)

        @pl.when(kj == 0)
        def _():
            qb_ref[...] = q_ref[...].astype(BF16)
            dq_ref[...] = jnp.zeros_like(dq_ref)

        @pl.when((hp == 0) & (kj == 0))
        def _():
            dc_ref[...] = jnp.zeros_like(dc_ref)

        masks = [_head_mask(0), _head_mask(1)]
        kb = k_ref[...].astype(BF16)
        vb = v_ref[...].astype(BF16)
        ctk = ct_ref[...]
        cks = [_sublane_row(ctk, 2 * hp + e) for e in range(2)]
        lane = lax.broadcasted_iota(jnp.int32, (BQ, LANES), 1)

        def step(qi, carry):
            dk, dv, cs = carry
            rows = pl.ds(pl.multiple_of(qi * BQ, BQ), BQ)
            qv = qb_ref[rows, :]
            dov = do_ref[rows, :]
            gt = g_ref[rows, :]
            lse2 = lse_ref[rows, :]
            dl2 = dl_ref[rows, :]
            dq = jnp.zeros((BQ, LANES), F32)
            dc = jnp.zeros((BQ, LANES), F32)
            cs_new = []
            for e in range(2):
                h = 2 * hp + e
                qe = jnp.where(masks[e], qv, jnp.zeros_like(qv))
                doe = jnp.where(masks[e], dov, jnp.zeros_like(dov))
                s = _fox_logits(qe, kb, _lane_col(gt, h), cks[e], qi, kj)
                p = jnp.exp(s - _lane_col(lse2, e))
                dp = _dot(doe, vb, 1, 1)
                ds = p * (dp - _lane_col(dl2, e))
                dv = dv + _dot(p, doe, 0, 0)
                dk = dk + _dot(ds, qe, 0, 0) * FOX_SCALE
                dq = dq + jnp.where(masks[e], _dot(ds, kb), 0.0) * FOX_SCALE
                cs_new.append(cs[e] + jnp.sum(ds, axis=0, keepdims=True))
                dc = dc + jnp.where(lane == h, jnp.sum(ds, axis=1, keepdims=True), 0.0)
            dq_ref[rows, :] += dq
            dc_ref[rows, :] += dc
            return dk, dv, tuple(cs_new)

        zero = jnp.zeros((BQ, LANES), F32)
        dk, dv, cs = lax.fori_loop(kj, NQ, step, (zero, zero, (jnp.zeros((1, BQ), F32), jnp.zeros((1, BQ), F32))))
        krows = pl.ds(pl.multiple_of(kj * BQ, BQ), BQ)
        dproj_ref[krows, LANES:2 * LANES] = dk.astype(BF16)
        dproj_ref[krows, 2 * LANES:3 * LANES] = dv.astype(BF16)
        r = lax.broadcasted_iota(jnp.int32, (BQ, BQ), 0)
        c = lax.broadcasted_iota(jnp.int32, (BQ, BQ), 1)
        dcol = jnp.zeros((BQ, LANES), F32)
        for e in range(2):
            col = jnp.sum(jnp.where(r == c, cs[e], 0.0), axis=1, keepdims=True)
            dcol = dcol + jnp.where(lane == 2 * hp + e, col, 0.0)
        dc_ref[krows, :] -= dcol

        @pl.when(kj == NQ - 1)
        def _():
            dproj_ref[:, 0:LANES] = dq_ref[...].astype(BF16)

    def blk(part):
        return pl.BlockSpec((BQ, LANES), lambda hp, j: (j, 3 * hp + part))

    pair = pl.BlockSpec((None, S, 2), lambda hp, j: (hp, 0, 0))
    return pl.pallas_call(
        body, name="fox_bwd", grid=(NPAIR, NQ),
        in_specs=[pl.BlockSpec((S, LANES), lambda hp, j: (0, 3 * hp)), blk(1), blk(2),
                  pl.BlockSpec((S, LANES), lambda hp, j: (0, hp)), pl.BlockSpec((S, LANES), lambda hp, j: (0, 0)),
                  pl.BlockSpec((None, 8, BQ), lambda hp, j: (j, 0, 0)), pair, pair],
        out_specs=[pl.BlockSpec((S, 3 * LANES), lambda hp, j: (0, hp)), pl.BlockSpec((S, LANES), lambda hp, j: (0, 0))],
        out_shape=[jax.ShapeDtypeStruct((S, DPROJ_PAD), BF16), jax.ShapeDtypeStruct((S, LANES), F32)],
        scratch_shapes=[pltpu.VMEM((S, LANES), BF16), pltpu.VMEM((S, LANES), F32)],
        compiler_params=_cparams(),
    )(proj, proj, proj, do, gates, gates_t, lse, delta)


NQKV = 3 * NGH
GDN_QSCALE = GHD ** -0.5


def _shift_down(x, s):
    if s == 0:
        return x
    row = lax.broadcasted_iota(jnp.int32, x.shape, 0)
    return jnp.where(row >= s, pltpu.roll(x, s, 0), 0.0)


def _shift_up(x, s):
    if s == 0:
        return x
    n = x.shape[0]
    row = lax.broadcasted_iota(jnp.int32, x.shape, 0)
    return jnp.where(row < n - s, pltpu.roll(x, n - s, 0), 0.0)


def _conv_pre(xv, wv):
    pre = xv * wv[CONV_K - 1:CONV_K, :]
    for j in range(CONV_K - 1):
        pre = pre + _shift_down(xv, CONV_K - 1 - j) * wv[j:j + 1, :]
    return pre


def _l2_factors(b):
    return b < 2 * NGH, jnp.where(b < NGH, GDN_QSCALE, 1.0)


def _gdn_pre(proj, conv_w):
    def body(x_ref, w_ref, o_ref):
        b = pl.program_id(0)
        c = _silu(_conv_pre(x_ref[...], w_ref[...]))
        normed, scale = _l2_factors(b)
        rs = lax.rsqrt(jnp.sum(c * c, axis=-1, keepdims=True) + EPS)
        o_ref[...] = c * jnp.where(normed, rs, 1.0) * scale

    return pl.pallas_call(
        body, name="gdn_pre", grid=(NQKV,),
        in_specs=[pl.BlockSpec((S, GHD), lambda b: (0, BLK_GDN + b)), pl.BlockSpec((CONV_K, GHD), lambda b: (0, b))],
        out_specs=pl.BlockSpec((S, GHD), lambda b: (0, b)),
        out_shape=jax.ShapeDtypeStruct((S, NQKV * GHD), F32), compiler_params=_cparams(),
    )(proj, conv_w)


def _gdn_pre_bwd(proj, conv_w, dqkv, dproj):
    def body(x_ref, w_ref, dy_ref, dproj_in, dx_ref, dw_ref):
        del dproj_in
        b = pl.program_id(0)
        xv = x_ref[...]
        wv = w_ref[...]
        pre = _conv_pre(xv, wv)
        sig = _sigmoid(pre)
        c = pre * sig
        normed, scale = _l2_factors(b)
        g = dy_ref[...] * scale
        rs = lax.rsqrt(jnp.sum(c * c, axis=-1, keepdims=True) + EPS)
        dc_n = rs * g - c * (rs * rs * rs) * jnp.sum(g * c, axis=-1, keepdims=True)
        dc = jnp.where(normed, dc_n, g)
        dpre = dc * sig * (1.0 + pre * (1.0 - sig))
        dx = dpre * wv[CONV_K - 1:CONV_K, :]
        for j in range(CONV_K - 1):
            dx = dx + _shift_up(dpre, CONV_K - 1 - j) * wv[j:j + 1, :]
        dx_ref[...] = dx.astype(BF16)
        for j in range(CONV_K):
            dw_ref[j:j + 1, :] = jnp.sum(dpre * _shift_down(xv, CONV_K - 1 - j), axis=0, keepdims=True)

    return pl.pallas_call(
        body, name="gdn_pre_bwd", grid=(NQKV,),
        in_specs=[pl.BlockSpec((S, GHD), lambda b: (0, BLK_GDN + b)), pl.BlockSpec((CONV_K, GHD), lambda b: (0, b)),
                  pl.BlockSpec((None, S, GHD), lambda b: (b // NGH, 0, b % NGH)), pl.BlockSpec(memory_space=pl.ANY)],
        out_specs=[pl.BlockSpec((S, GHD), lambda b: (0, BLK_GDN + b)), pl.BlockSpec((CONV_K, GHD), lambda b: (0, b))],
        out_shape=[jax.ShapeDtypeStruct((S, DPROJ_PAD), BF16), jax.ShapeDtypeStruct((CONV_K, NQKV * GHD), F32)],
        input_output_aliases={3: 0}, compiler_params=_cparams(),
    )(proj, conv_w, dqkv, dproj)


CB = 4
NCB = NCH // CB


def _chunk_prep(qs, ks, vs, gcols, bcols, t_saved=None):
    n = range(len(qs))
    r = lax.broadcasted_iota(jnp.int32, (CHUNK, CHUNK), 0)
    c = lax.broadcasted_iota(jnp.int32, (CHUNK, CHUNK), 1)
    incl = c <= r
    eye = (r == c).astype(F32)
    grow = [jnp.sum(gcols[i] * eye, axis=0, keepdims=True) for i in n]
    gc_col = [jnp.sum(jnp.where(incl, grow[i], 0.0), axis=1, keepdims=True) for i in n]
    gc_row = [jnp.sum(jnp.where(r <= c, gcols[i], 0.0), axis=0, keepdims=True) for i in n]
    decay = [jnp.exp(jnp.where(incl, gc_col[i] - gc_row[i], -jnp.inf)) for i in n]
    kb = [ks[i] * bcols[i] for i in n]
    vb = [vs[i] * bcols[i] for i in n]
    kk = [_mm_nt(kb[i], ks[i]) for i in n]
    m = [jnp.where(c < r, kk[i] * decay[i], 0.0) for i in n]
    if t_saved is None:
        t_inv = [eye - m[i] for i in n]
        p = [_hdot(m[i], m[i]) for i in n]
        for step in range(5):
            t_inv = [t_inv[i] + _hdot(t_inv[i], p[i]) for i in n]
            if step < 4:
                p = [_hdot(p[i], p[i]) for i in n]
    else:
        t_inv = [_saved_inverse(m[i], t_saved[i]) for i in n]
    egc = [jnp.exp(gc_col[i]) for i in n]
    u = [_mm_nn(t_inv[i], vb[i]) for i in n]
    w = [_mm_nn(t_inv[i], kb[i] * egc[i]) for i in n]
    qk = [_mm_nt(qs[i], ks[i]) for i in n]
    gc_last = [gc_col[i][CHUNK - 1:CHUNK, :] for i in n]
    return [(u[i], w[i], qk[i] * decay[i], qs[i] * egc[i], ks[i] * jnp.exp(gc_last[i] - gc_col[i]), jnp.exp(gc_last[i]),
             t_inv[i]) for i in n]


def _prep_specs():
    rows = CB * CHUNK
    qs = pl.BlockSpec((rows, GHD), lambda i, h: (i, h))
    ks = pl.BlockSpec((rows, GHD), lambda i, h: (i, NGH + h))
    vs = pl.BlockSpec((rows, GHD), lambda i, h: (i, 2 * NGH + h))
    gs = pl.BlockSpec((rows, LANES), lambda i, h: (i, 0))
    a_s = pl.BlockSpec((None, rows, CHUNK), lambda i, h: (h, i, 0))
    gl_s = pl.BlockSpec((None, CB, 1, LANES), lambda i, h: (h, i, 0, 0))
    return qs, ks, vs, gs, a_s, gl_s


def _gdn_prep(qkv, gates):
    def body(q_ref, k_ref, v_ref, g_ref, u_ref, w_ref, qd_ref, kd_ref, a_ref, gl_ref, t_ref):
        h = pl.program_id(1)
        chunks = [pl.ds(cidx * CHUNK, CHUNK) for cidx in range(CB)]
        gts = [g_ref[rows, :] for rows in chunks]
        outs = _chunk_prep([q_ref[rows, :] for rows in chunks], [k_ref[rows, :] for rows in chunks],
                           [v_ref[rows, :] for rows in chunks], [_lane_col(gt, LANE_G + h) for gt in gts],
                           [_lane_col(gt, LANE_BETA + h) for gt in gts])
        for cidx, rows in enumerate(chunks):
            u, w, a, qd, kd, gl, t_inv = outs[cidx]
            u_ref[rows, :] = u
            w_ref[rows, :] = w
            qd_ref[rows, :] = qd
            kd_ref[rows, :] = kd
            a_ref[rows, :] = a
            t_ref[rows, :] = t_inv
            gl_ref[cidx] = jnp.broadcast_to(gl, (1, LANES))

    qs, ks, vs, gs, a_s, gl_s = _prep_specs()
    tok = jax.ShapeDtypeStruct((S, DGDN), F32)
    sq = jax.ShapeDtypeStruct((NGH, S, CHUNK), F32)
    return pl.pallas_call(
        body, name="gdn_prep", grid=(NCB, NGH), in_specs=[qs, ks, vs, gs], out_specs=[qs, qs, qs, qs, a_s, gl_s, a_s],
        out_shape=[tok, tok, tok, tok, sq, jax.ShapeDtypeStruct((NGH, NCH, 1, LANES), F32), sq],
        compiler_params=_cparams(),
    )(qkv, qkv, qkv, gates)


def _gdn_prep_bwd(qkv, gates, t_inv, du, dw, dqd, dkd, da, dgl):
    def body(q_ref, k_ref, v_ref, g_ref, t_ref, du_ref, dw_ref, dqd_ref, dkd_ref, da_ref, dgl_ref, dqkv_ref, dg_ref):
        h = pl.program_id(1)

        @pl.when(h == 0)
        def _():
            dg_ref[...] = jnp.zeros_like(dg_ref)

        lane = lax.broadcasted_iota(jnp.int32, (CHUNK, LANES), 1)
        chunks = [pl.ds(cidx * CHUNK, CHUNK) for cidx in range(CB)]
        gts = [g_ref[rows, :] for rows in chunks]
        t_saved = [t_ref[rows, :] for rows in chunks]
        _, vjp = jax.vjp(lambda *args: [o[:6] for o in _chunk_prep(*args, t_saved=t_saved)],
                         [q_ref[rows, :] for rows in chunks], [k_ref[rows, :] for rows in chunks],
                         [v_ref[rows, :] for rows in chunks], [_lane_col(gt, LANE_G + h) for gt in gts],
                         [_lane_col(gt, LANE_BETA + h) for gt in gts])
        dqs, dks, dvs, dgcs, dbcs = vjp([(du_ref[rows, :], dw_ref[rows, :], da_ref[rows, :], dqd_ref[rows, :],
                                          dkd_ref[rows, :], dgl_ref[cidx][:, 0:1]) for cidx, rows in enumerate(chunks)])
        for cidx, rows in enumerate(chunks):
            dq, dk, dv, dgc, dbc = dqs[cidx], dks[cidx], dvs[cidx], dgcs[cidx], dbcs[cidx]
            dqkv_ref[0, rows, :] = dq
            dqkv_ref[1, rows, :] = dk
            dqkv_ref[2, rows, :] = dv
            dg_ref[rows, :] += jnp.where(lane == LANE_G + h, dgc, 0.0) + jnp.where(lane == LANE_BETA + h, dbc, 0.0)

    qs, ks, vs, gs, a_s, gl_s = _prep_specs()
    return pl.pallas_call(
        body, name="gdn_prep_bwd", grid=(NCB, NGH), in_specs=[qs, ks, vs, gs, a_s, qs, qs, qs, qs, a_s, gl_s],
        out_specs=[pl.BlockSpec((3, CB * CHUNK, GHD), lambda i, h: (0, i, h)), gs],
        out_shape=[jax.ShapeDtypeStruct((3, S, DGDN), F32), jax.ShapeDtypeStruct((S, LANES), F32)],
        compiler_params=_cparams(),
    )(qkv, qkv, qkv, gates, t_inv, du, dw, dqd, dkd, da, dgl)


def _scan_specs():
    hs = pl.BlockSpec((S, GHD), lambda h: (0, h))
    a_s = pl.BlockSpec((None, S, CHUNK), lambda h: (h, 0, 0))
    gl_s = pl.BlockSpec((None, NCH, 1, LANES), lambda h: (h, 0, 0, 0))
    st_s = pl.BlockSpec((None, NCH, GHD, GHD), lambda h: (h, 0, 0, 0))
    gz_s = pl.BlockSpec((S, GHD), lambda h: (0, BLK_GZ + h))
    mix_s = pl.BlockSpec((S, GHD), lambda h: (0, NPAIR + h))
    return hs, a_s, gl_s, st_s, gz_s, mix_s


def _gdn_scan(u, w, qd, kd, a, gl, proj, w_norm, mix):
    def body(u_ref, w_ref, qd_ref, kd_ref, a_ref, gl_ref, z_ref, wn_ref, mix_in, mix_ref, o_ref, st_ref):
        del mix_in

        def step(ci, state):
            rows = pl.ds(pl.multiple_of(ci * CHUNK, CHUNK), CHUNK)
            st_ref[ci] = state
            vn = u_ref[rows, :] - _dot(w_ref[rows, :], state)
            o_ref[rows, :] = _dot(qd_ref[rows, :], state) + _dot(a_ref[rows, :], vn)
            return state * gl_ref[ci] + _dot(kd_ref[rows, :], vn, 0, 0)

        lax.fori_loop(0, NCH, step, jnp.zeros((GHD, GHD), F32))
        ov = o_ref[...]
        mix_ref[...] = (ov * _rms_scale(ov) * wn_ref[...] * _silu(z_ref[...])).astype(BF16)

    hs, a_s, gl_s, st_s, gz_s, mix_s = _scan_specs()
    return pl.pallas_call(
        body, name="gdn_scan", grid=(NGH,),
        in_specs=[hs, hs, hs, hs, a_s, gl_s, gz_s, pl.BlockSpec((1, GHD), lambda h: (0, 0)), pl.BlockSpec(memory_space=pl.ANY)],
        out_specs=[mix_s, hs, st_s],
        out_shape=[jax.ShapeDtypeStruct((S, D), BF16), jax.ShapeDtypeStruct((S, DGDN), F32),
                   jax.ShapeDtypeStruct((NGH, NCH, GHD, GHD), F32)],
        input_output_aliases={8: 0}, compiler_params=_cparams(),
    )(u, w, qd, kd, a, gl, proj, w_norm, mix)


def _gdn_scan_bwd(dmix, o, proj, w_norm, u, w, qd, kd, a, gl, states, dproj):
    def body(dy_ref, o_ref, z_ref, wn_ref, u_ref, w_ref, qd_ref, kd_ref, a_ref, gl_ref, st_ref, dproj_in,
             dz_ref, du_ref, dw_ref, dqd_ref, dkd_ref, da_ref, dgl_ref, dwn_ref, do_ref):
        del dproj_in
        h = pl.program_id(0)
        ov = o_ref[...]
        zv = z_ref[...]
        wn = wn_ref[...]
        g = dy_ref[...]
        sig = _sigmoid(zv)
        dz_ref[...] = (g * (ov * _rms_scale(ov) * wn) * sig * (1.0 + zv * (1.0 - sig))).astype(BF16)
        do, dwt = _rms_bwd(ov, wn, g * zv * sig)
        do_ref[...] = do

        @pl.when(h == 0)
        def _():
            dwn_ref[...] = jnp.zeros_like(dwn_ref)

        dwn_ref[...] += jnp.sum(dwt, axis=0, keepdims=True)

        def step(t, dstate):
            ci = NCH - 1 - t
            rows = pl.ds(pl.multiple_of(ci * CHUNK, CHUNK), CHUNK)
            state = st_ref[ci]
            dov = do_ref[rows, :]
            wv = w_ref[rows, :]
            kdv = kd_ref[rows, :]
            vn = u_ref[rows, :] - _dot(wv, state)
            dvn = _dot(a_ref[rows, :], dov, 0, 0) + _dot(kdv, dstate)
            da_ref[rows, :] = _dot(dov, vn, 1, 1)
            dqd_ref[rows, :] = _dot(dov, state, 1, 1)
            dkd_ref[rows, :] = _dot(vn, dstate, 1, 1)
            dgl = jnp.sum(jnp.sum(dstate * state, axis=1, keepdims=True), axis=0, keepdims=True)
            dgl_ref[ci] = jnp.broadcast_to(dgl, (1, LANES))
            du_ref[rows, :] = dvn
            dw_ref[rows, :] = -_dot(dvn, state, 1, 1)
            return dstate * gl_ref[ci] + _dot(qd_ref[rows, :], dov, 0, 0) - _dot(wv, dvn, 0, 0)

        lax.fori_loop(0, NCH, step, jnp.zeros((GHD, GHD), F32))

    hs, a_s, gl_s, st_s, gz_s, mix_s = _scan_specs()
    vec = pl.BlockSpec((1, GHD), lambda h: (0, 0))
    tok = jax.ShapeDtypeStruct((S, DGDN), F32)
    return pl.pallas_call(
        body, name="gdn_scan_bwd", grid=(NGH,),
        in_specs=[mix_s, hs, gz_s, vec, hs, hs, hs, hs, a_s, gl_s, st_s, pl.BlockSpec(memory_space=pl.ANY)],
        out_specs=[gz_s, hs, hs, hs, hs, a_s, gl_s, vec],
        out_shape=[jax.ShapeDtypeStruct((S, DPROJ_PAD), BF16), tok, tok, tok, tok,
                   jax.ShapeDtypeStruct((NGH, S, CHUNK), F32), jax.ShapeDtypeStruct((NGH, NCH, 1, LANES), F32),
                   jax.ShapeDtypeStruct((1, GHD), F32)],
        input_output_aliases={11: 0}, scratch_shapes=[pltpu.VMEM((S, GHD), F32)], compiler_params=_cparams(),
    )(dmix, o, proj, w_norm, u, w, qd, kd, a, gl, states, dproj)


def _place():
    return lax.axis_index("x"), lax.axis_index("y"), lax.axis_index("c")


def _other_chips(x, y):
    return [(1 - x, y), (x, 1 - y), (1 - x, 1 - y)]


HBM = pl.BlockSpec(memory_space=pltpu.HBM)
VMEM = pl.BlockSpec(memory_space=pltpu.VMEM)


def _half_rows(ref_or_rows, half):
    rows = ref_or_rows // 2
    return pl.ds(pl.multiple_of(half * rows, rows), rows)


def _weights_allgather(shards, conv):
    n = len(shards)

    def body(*refs):
        src, conv_src = refs[:n], refs[n]
        outs, conv_out = refs[n + 1:2 * n + 1], refs[2 * n + 1]
        send_sems, recv_sems = refs[2 * n + 2:]
        x, y, c = _place()
        me_chip = 2 * x + y
        sibling = (x, y, 1 - c)
        chips = _other_chips(x, y)
        chip_ids = [2 * ch[0] + ch[1] for ch in chips]

        def copy(a, k, chip_index, half, to, from_src):
            rows = _half_rows(src[a].shape[0], half)
            dst = outs[a].at[chip_index, rows]
            return pltpu.make_async_remote_copy(
                src_ref=src[a].at[rows] if from_src else dst, dst_ref=dst, send_sem=send_sems.at[6 * a + k],
                recv_sem=recv_sems.at[6 * a + k], device_id=to, device_id_type=MESH)

        def conv_copy(k, chip_index, to):
            return pltpu.make_async_remote_copy(
                src_ref=conv_src, dst_ref=conv_out.at[chip_index], send_sem=send_sems.at[6 * n + k],
                recv_sem=recv_sems.at[6 * n + k], device_id=to, device_id_type=MESH)

        first = [copy(a, j, me_chip, c, (*chips[j], c), True) for a in range(n) for j in range(3)]
        first += [conv_copy(j, me_chip, (*chips[j], c)) for j in range(3)]
        for cp in first:
            cp.start()
        passed = []
        for a in range(n):
            for j in range(3):
                copy(a, j, chip_ids[j], c, (x, y, c), False).wait_recv()
                cp = copy(a, 3 + j, chip_ids[j], c, sibling, False)
                cp.start()
                passed.append(cp)
        for a in range(n):
            for j in range(3):
                copy(a, 3 + j, chip_ids[j], 1 - c, (x, y, c), False).wait_recv()
        for j in range(3):
            conv_copy(j, chip_ids[j], (x, y, c)).wait_recv()
        for cp in first + passed:
            cp.wait_send()

    n_sem = 6 * n + 3
    return pl.pallas_call(
        body, name="weights_allgather", in_specs=[HBM] * (n + 1), out_specs=[HBM] * (n + 1),
        out_shape=[jax.ShapeDtypeStruct((NCHIP,) + s.shape, s.dtype) for s in shards]
        + [jax.ShapeDtypeStruct((NCHIP,) + conv.shape, conv.dtype)],
        scratch_shapes=[pltpu.SemaphoreType.DMA((n_sem,)), pltpu.SemaphoreType.DMA((n_sem,))],
        compiler_params=_cparams(),
    )(*shards, conv)


def _with_own(gathered, own):
    x, y, _ = _place()
    return lax.dynamic_update_index_in_dim(gathered, own, 2 * x + y, axis=0)


def _grads_pair_exchange(grads):
    n = len(grads)

    def body(*refs):
        src, theirs = refs[:n], refs[n:2 * n]
        send_sems, recv_sems = refs[2 * n:]
        x, y, c = _place()
        copies = []
        for a in range(n):
            cp = pltpu.make_async_remote_copy(
                src_ref=src[a].at[:, _half_rows(src[a].shape[1], 1 - c)], dst_ref=theirs[a], send_sem=send_sems.at[a],
                recv_sem=recv_sems.at[a], device_id=(x, y, 1 - c), device_id_type=MESH)
            cp.start()
            copies.append(cp)
        for cp in copies:
            cp.wait()

    return pl.pallas_call(
        body, name="grads_pair_exchange", in_specs=[HBM] * n, out_specs=[HBM] * n,
        out_shape=[jax.ShapeDtypeStruct((g.shape[0], g.shape[1] // 2, g.shape[2]), g.dtype) for g in grads],
        scratch_shapes=[pltpu.SemaphoreType.DMA((n,)), pltpu.SemaphoreType.DMA((n,))],
        compiler_params=_cparams(),
    )(*grads)


def _pair_sum(grads, theirs):
    n = len(grads)

    def body(*refs):
        south = lax.axis_index("c") == 0
        for a in range(n):
            g = refs[a][...]
            half = g.shape[0] // 2
            mine = jnp.where(south, g[:half], g[half:])
            refs[2 * n + a][...] = (mine.astype(F32) + refs[n + a][...].astype(F32)).astype(BF16)

    def specs(arrs):
        return [pl.BlockSpec((None,) + g.shape[1:], lambda j: (j, 0, 0)) for g in arrs]

    return pl.pallas_call(
        body, name="grads_pair_sum", grid=(NCHIP,), in_specs=specs(grads) + specs(theirs), out_specs=specs(theirs),
        out_shape=[jax.ShapeDtypeStruct(g.shape, BF16) for g in theirs], compiler_params=_cparams(),
    )(*grads, *theirs)


def _grads_chip_exchange(parts):
    n = len(parts)

    def body(*refs):
        src, outs = refs[:n], refs[n:2 * n]
        send_sems, recv_sems = refs[2 * n:]
        x, y, c = _place()
        copies = []
        for a in range(n):
            for k, chip in enumerate(_other_chips(x, y)):
                cp = pltpu.make_async_remote_copy(
                    src_ref=src[a].at[2 * chip[0] + chip[1]], dst_ref=outs[a].at[k], send_sem=send_sems.at[3 * a + k],
                    recv_sem=recv_sems.at[3 * a + k], device_id=(*chip, c), device_id_type=MESH)
                cp.start()
                copies.append(cp)
        for cp in copies:
            cp.wait()

    return pl.pallas_call(
        body, name="grads_chip_exchange", in_specs=[HBM] * n, out_specs=[HBM] * n,
        out_shape=[jax.ShapeDtypeStruct((NCHIP - 1,) + p.shape[1:], p.dtype) for p in parts],
        scratch_shapes=[pltpu.SemaphoreType.DMA((3 * n,)), pltpu.SemaphoreType.DMA((3 * n,))],
        compiler_params=_cparams(),
    )(*parts)


def _chip_sum(parts, received):
    n = len(parts)
    steps = 4

    def body(*refs):
        chip = 2 * lax.axis_index("x") + lax.axis_index("y")
        for a in range(n):
            p, r = refs[a], refs[n + a]
            own = jnp.where(chip == 0, p[0], jnp.where(chip == 1, p[1], jnp.where(chip == 2, p[2], p[3])))
            refs[2 * n + a][...] = ((own.astype(F32) + r[0].astype(F32)) + r[1].astype(F32)) + r[2].astype(F32)

    def specs(arrs):
        return [pl.BlockSpec((g.shape[0], g.shape[1] // steps, g.shape[2]), lambda i: (0, i, 0)) for g in arrs]

    out_specs = [pl.BlockSpec((g.shape[1] // steps, g.shape[2]), lambda i: (i, 0)) for g in parts]
    return pl.pallas_call(
        body, name="grads_chip_sum", grid=(steps,), in_specs=specs(parts) + specs(received), out_specs=out_specs,
        out_shape=[jax.ShapeDtypeStruct(g.shape[1:], F32) for g in parts], compiler_params=_cparams(),
    )(*parts, *received)


def _grads_pair_share(halves):
    n = len(halves)

    def body(*refs):
        src, outs = refs[:n], refs[n:2 * n]
        send_sems, recv_sems = refs[2 * n:]
        x, y, c = _place()
        copies = []
        for a in range(n):
            cp = pltpu.make_async_remote_copy(
                src_ref=src[a], dst_ref=outs[a], send_sem=send_sems.at[a], recv_sem=recv_sems.at[a],
                device_id=(x, y, 1 - c), device_id_type=MESH)
            cp.start()
            copies.append(cp)
        for cp in copies:
            cp.wait()

    return pl.pallas_call(
        body, name="grads_pair_share", in_specs=[HBM] * n, out_specs=[HBM] * n,
        out_shape=[jax.ShapeDtypeStruct(h.shape, F32) for h in halves],
        scratch_shapes=[pltpu.SemaphoreType.DMA((n,)), pltpu.SemaphoreType.DMA((n,))],
        compiler_params=_cparams(),
    )(*halves)


def _adamw_math(w, g, m, v):
    nm = ADAM_B1 * m + (1.0 - ADAM_B1) * g
    nv = ADAM_B2 * v + (1.0 - ADAM_B2) * jnp.square(g)
    m_hat = nm / (1.0 - ADAM_B1 ** ADAM_STEP)
    v_hat = nv / (1.0 - ADAM_B2 ** ADAM_STEP)
    return -ADAM_LR * (m_hat / (jnp.sqrt(v_hat) + ADAM_EPS) + ADAM_WD * w), nm, nv


def _adamw_big(ws, g_mine, g_theirs, ms, vs):
    n = len(ws)
    steps = 8

    def body(*refs):
        own_half = (pl.program_id(0) // (steps // 2)) == lax.axis_index("c")
        for a in range(n):
            w = refs[a][...]
            g = jnp.where(own_half, refs[n + a][...], refs[2 * n + a][...])[:, :w.shape[1]]
            d, nm, nv = _adamw_math(w, g, refs[3 * n + a][...], refs[4 * n + a][...])
            refs[5 * n + a][...] = g
            refs[6 * n + a][...] = d
            refs[7 * n + a][...] = nm
            refs[8 * n + a][...] = nv

    specs = [pl.BlockSpec((w.shape[0] // steps, w.shape[1]), lambda i: (i, 0)) for w in ws]
    half_specs = [pl.BlockSpec((g.shape[0] // (steps // 2), g.shape[1]), lambda i: (i % (steps // 2), 0)) for g in g_mine]
    shapes = [jax.ShapeDtypeStruct(w.shape, F32) for w in ws]
    res = pl.pallas_call(
        body, name="adamw_big", grid=(steps,), in_specs=specs + half_specs * 2 + specs * 2, out_specs=specs * 4,
        out_shape=shapes * 4, compiler_params=_cparams(),
    )(*ws, *g_mine, *g_theirs, *ms, *vs)
    return res[:n], res[n:2 * n], res[2 * n:3 * n], res[3 * n:]


NORM_NAMES = ("pre_mix_norm", "post_mix_norm", "pre_mlp_norm", "post_mlp_norm")
SMALL_NAMES = NORM_NAMES + ("gdn_conv_w", "fox_f_bias", "gdn_dt_bias", "gdn_a_log", "fox_out_norm", "gdn_out_norm")
CONV_COLS = 3 * DGDN // NCHIP


def _small_allreduce(d_norms, d_conv, sums, d_fox_norm, d_gdn_norm):
    def body(*refs):
        dn_ref, dconv_ref, sums_ref, dfn_ref, dgn_ref = refs[:5]
        outs = refs[5:10]
        g_norms, g_conv, g_sums, g_fn, g_gn, send_sems, recv_sems, local_sem = refs[10:]
        x, y, c = _place()
        me = 4 * x + 2 * y + c
        g_norms[me] = dn_ref[...]
        g_sums[me] = sums_ref[...]
        g_fn[me] = dfn_ref[...]
        g_gn[me] = dgn_ref[...]

        def conv_cols(chip_index):
            return dconv_ref.at[:, pl.ds(pl.multiple_of(chip_index * CONV_COLS, LANES), CONV_COLS)]

        own = pltpu.make_async_copy(conv_cols(2 * x + y), g_conv.at[me], local_sem)
        own.start()
        copies = []
        for k in range(1, NDEV):
            px, py, pc = x ^ ((k >> 2) & 1), y ^ ((k >> 1) & 1), c ^ (k & 1)
            pairs = [(dn_ref, g_norms), (conv_cols(2 * px + py), g_conv), (sums_ref, g_sums), (dfn_ref, g_fn), (dgn_ref, g_gn)]
            for a, (src, dst) in enumerate(pairs):
                cp = pltpu.make_async_remote_copy(
                    src_ref=src, dst_ref=dst.at[me], send_sem=send_sems.at[5 * (k - 1) + a],
                    recv_sem=recv_sems.at[5 * (k - 1) + a], device_id=(px, py, pc), device_id_type=MESH)
                cp.start()
                copies.append(cp)
        own.wait()
        for cp in copies:
            cp.wait()

        def total(buf):
            acc = buf[0]
            for i in range(1, NDEV):
                acc = acc + buf[i]
            return acc

        for out, buf in zip(outs, (g_norms, g_conv, g_sums, g_fn, g_gn)):
            out[...] = total(buf)

    n_sem = 5 * (NDEV - 1)
    shapes = [(4, D), (CONV_K, CONV_COLS), (8, LANES), (1, LANES), (1, LANES)]
    return pl.pallas_call(
        body, name="small_allreduce", in_specs=[VMEM] * 5, out_specs=[VMEM] * 5,
        out_shape=[jax.ShapeDtypeStruct(s, F32) for s in shapes],
        scratch_shapes=[pltpu.VMEM((NDEV,) + s, F32) for s in shapes]
        + [pltpu.SemaphoreType.DMA((n_sem,)), pltpu.SemaphoreType.DMA((n_sem,)), pltpu.SemaphoreType.DMA],
        compiler_params=_cparams(),
    )(d_norms, d_conv, sums, d_fox_norm, d_gdn_norm)


def _small_adamw(totals, ws, ms, vs):
    n = len(SMALL_NAMES)

    def body(*refs):
        t_norms, t_conv, t_sums, t_fn, t_gn = [r[...] for r in refs[:5]]
        w_refs, m_refs, v_refs = refs[5:5 + n], refs[5 + n:5 + 2 * n], refs[5 + 2 * n:5 + 3 * n]
        outs = refs[5 + 3 * n:]
        grads = [t_norms[i:i + 1, :] for i in range(4)] + [
            t_conv, t_sums[0:1, 0:NFH], t_sums[1:2, 0:NGH], t_sums[2:3, 0:NGH], t_fn[:, 0:FHD], t_gn]
        for a in range(n):
            d, nm, nv = _adamw_math(w_refs[a][...], grads[a], m_refs[a][...], v_refs[a][...])
            outs[a][...] = grads[a]
            outs[n + a][...] = d
            outs[2 * n + a][...] = nm
            outs[3 * n + a][...] = nv

    def whole(arr):
        return pl.BlockSpec(arr.shape, lambda i: (0, 0))

    res = pl.pallas_call(
        body, name="small_adamw", grid=(1,), in_specs=[whole(t) for t in totals] + [whole(w) for w in ws] * 3,
        out_specs=[whole(w) for w in ws] * 4, out_shape=[jax.ShapeDtypeStruct(w.shape, F32) for w in ws] * 4,
        compiler_params=_cparams(),
    )(*totals, *ws, *ms, *vs)
    return res[:n], res[n:2 * n], res[2 * n:3 * n], res[3 * n:]


def _to_padded_cols(w):
    pieces = [w[:, part * DFOX + hp * LANES:part * DFOX + (hp + 1) * LANES] for hp in range(NPAIR) for part in range(3)]
    pieces += [w[:, 1544:3080], w[:, 3088:3600], w[:, 1536:1544], w[:, 3080:3088],
               jnp.zeros((w.shape[0], 2 * LANES - 16), w.dtype)]
    return jnp.concatenate(pieces, axis=1)


def _from_padded_cols(w):
    c0 = BLK_SMALL * LANES
    fox = [w[:, (3 * hp + part) * LANES:(3 * hp + part + 1) * LANES] for part in range(3) for hp in range(NPAIR)]
    return jnp.concatenate(fox + [w[:, c0:c0 + 8], w[:, BLK_GDN * LANES:BLK_GZ * LANES], w[:, c0 + 8:c0 + 16],
                                  w[:, BLK_GZ * LANES:BLK_SMALL * LANES]], axis=1)


def _local_step(x, target, win_p, wout, wup3, wdown, pre_mix_norm, fox_f_bias, fox_out_norm, conv_w, gdn_a_log,
                gdn_dt_bias, gdn_out_norm, post_mix_norm, pre_mlp_norm, post_mlp_norm):
    bias_vec = jnp.zeros((1, LANES), F32).at[0, 0:NFH].set(fox_f_bias).at[0, LANE_G:LANE_G + NGH].set(gdn_dt_bias)
    alog_vec = jnp.zeros((1, LANES), F32).at[0, LANE_G:LANE_G + NGH].set(gdn_a_log)
    w2 = jnp.concatenate([fox_out_norm, fox_out_norm], axis=1)

    h = _pre_norm(x, pre_mix_norm)
    proj = _matmul(h, win_p, tm=2048, tn=768, tk=1024, name="mm_proj")
    gates, gates_t = _gates(proj, bias_vec, alog_vec)
    mix, fox_o, lse = _fox_fwd(proj, gates, gates_t, w2)
    qkv = _gdn_pre(proj, conv_w)
    u, w, qd, kd, a_intra, gl, t_inv = _gdn_prep(qkv, gates)
    mix, gdn_raw, states = _gdn_scan(u, w, qd, kd, a_intra, gl, proj, gdn_out_norm, mix)
    mixed = _matmul(mix, wout, tm=2048, tk=1024, name="mm_out")
    x1, h2 = _post_mix(x, mixed, post_mix_norm, pre_mlp_norm)

    def relu2(acc):
        r = jnp.maximum(acc, 0.0)
        return acc, r * r

    up, act = _matmul(h2, wup3, b3=True, tm=1024, tn=1024, tk=1024, out_dtypes=(F32, BF16), epilogue=relu2, name="mm_up")
    y = _matmul(act, wdown, tm=2048, tk=1024, name="mm_down")
    dx2, dy, d_post_mlp, loss_row = _loss_head(x1, y, post_mlp_norm, target)

    dwdown = _matmul(act, dy, ta=True, tm=1024, tn=1024, tk=2048, out_dtypes=(BF16,), name="mm_dwdown")

    def relu2_bwd(acc, upv):
        return (acc * 2.0 * jnp.maximum(upv, 0.0),)

    dup = _matmul(dy, wdown, tb=True, tm=1024, tn=1024, tk=1024, out_dtypes=(BF16,), extra=(up,), epilogue=relu2_bwd,
                  name="mm_dact")
    dwup3 = _matmul(h2, dup, ta=True, tm=1024, tn=1024, tk=2048, out_dtypes=(BF16,), o3=True, name="mm_dwup")
    dh2 = _matmul(dup, wup3, tb=True, b3=True, tm=2048, tk=1024, name="mm_dh2")
    dx1, dmixed, d_pre_mlp, d_post_mix = _mid_bwd(dh2, x1, pre_mlp_norm, dx2, mixed, post_mix_norm)
    dwout = _matmul(mix, dmixed, ta=True, tm=1024, tn=1024, tk=2048, out_dtypes=(BF16,), name="mm_dwout")
    dmix = _matmul(dmixed, wout, tb=True, tm=2048, tk=1024, name="mm_dmix")

    dfox, delta, d_fox_norm = _fox_norm_bwd(fox_o, dmix, w2)
    dproj, dcum_fox = _fox_bwd(proj, dfox, gates, gates_t, lse, delta)
    dproj, du, dw, dqd, dkd, da, dgl, d_gdn_norm = _gdn_scan_bwd(dmix, gdn_raw, proj, gdn_out_norm, u, w, qd, kd,
                                                                 a_intra, gl, states, dproj)
    dqkv, dgates_gdn = _gdn_prep_bwd(qkv, gates, t_inv, du, dw, dqd, dkd, da, dgl)
    dproj, d_conv = _gdn_pre_bwd(proj, conv_w, dqkv, dproj)
    dproj, sums = _gates_bwd(proj, bias_vec, alog_vec, dgates_gdn, dcum_fox, dproj)

    dwin_p = _matmul(h, dproj, ta=True, tm=1024, tn=1280, tk=2048, out_dtypes=(BF16,), name="mm_dwin")
    dh = _matmul(dproj, win_p, tb=True, tm=2048, tk=1280, name="mm_dh")
    grad_x, d_pre_mix = _pre_norm_bwd(dh, x, pre_mix_norm, dx1)

    d_norms = jnp.concatenate([d_pre_mix, d_post_mix, d_pre_mlp, d_post_mlp], axis=0)
    return loss_row[0, 0], grad_x, (dwin_p, dwout, dwup3, dwdown), (d_norms, d_conv, sums, d_fox_norm, d_gdn_norm)


def kernel(x, pre_mix_norm, w_in, fox_f_bias, fox_out_norm, gdn_conv_w, gdn_a_log, gdn_dt_bias, gdn_out_norm, w_out, post_mix_norm, pre_mlp_norm, w_up, w_down, post_mlp_norm, loss_target, m_pre_mix_norm, m_w_in, m_fox_f_bias, m_fox_out_norm, m_gdn_conv_w, m_gdn_a_log, m_gdn_dt_bias, m_gdn_out_norm, m_w_out, m_post_mix_norm, m_pre_mlp_norm, m_w_up, m_w_down, m_post_mlp_norm, v_pre_mix_norm, v_w_in, v_fox_f_bias, v_fox_out_norm, v_gdn_conv_w, v_gdn_a_log, v_gdn_dt_bias, v_gdn_out_norm, v_w_out, v_post_mix_norm, v_pre_mlp_norm, v_w_up, v_w_down, v_post_mlp_norm):
    weights = dict(pre_mix_norm=pre_mix_norm, w_in=w_in, fox_f_bias=fox_f_bias, fox_out_norm=fox_out_norm, gdn_conv_w=gdn_conv_w,
                   gdn_a_log=gdn_a_log, gdn_dt_bias=gdn_dt_bias, gdn_out_norm=gdn_out_norm, w_out=w_out, post_mix_norm=post_mix_norm,
                   pre_mlp_norm=pre_mlp_norm, w_up=w_up, w_down=w_down, post_mlp_norm=post_mlp_norm)
    m_in = dict(pre_mix_norm=m_pre_mix_norm, w_in=m_w_in, fox_f_bias=m_fox_f_bias, fox_out_norm=m_fox_out_norm, gdn_conv_w=m_gdn_conv_w,
                gdn_a_log=m_gdn_a_log, gdn_dt_bias=m_gdn_dt_bias, gdn_out_norm=m_gdn_out_norm, w_out=m_w_out, post_mix_norm=m_post_mix_norm,
                pre_mlp_norm=m_pre_mlp_norm, w_up=m_w_up, w_down=m_w_down, post_mlp_norm=m_post_mlp_norm)
    v_in = dict(pre_mix_norm=v_pre_mix_norm, w_in=v_w_in, fox_f_bias=v_fox_f_bias, fox_out_norm=v_fox_out_norm, gdn_conv_w=v_gdn_conv_w,
                gdn_a_log=v_gdn_a_log, gdn_dt_bias=v_gdn_dt_bias, gdn_out_norm=v_gdn_out_norm, w_out=v_w_out, post_mix_norm=v_post_mix_norm,
                pre_mlp_norm=v_pre_mlp_norm, w_up=v_w_up, w_down=v_w_down, post_mlp_norm=v_post_mlp_norm)
    order_w = ("pre_mix_norm", "w_in", "fox_f_bias", "fox_out_norm", "gdn_conv_w", "gdn_a_log", "gdn_dt_bias", "gdn_out_norm", "w_out",
               "post_mix_norm", "pre_mlp_norm", "w_up", "w_down", "post_mlp_norm")
    big = ("w_in", "w_out", "w_up", "w_down")

    def row(v):
        return v if v.ndim == 2 else v.reshape(1, -1)

    cw = DPROJ // NCHIP
    shards = [jnp.pad(w_in.astype(BF16), ((0, 0), (0, D - cw)))] + [weights[n].astype(BF16) for n in big[1:]]
    gathered = _weights_allgather(shards, gdn_conv_w)
    win_g, wout_g, wup3, wdown_g, conv_g = [_with_own(g, own) for g, own in zip(gathered, shards + [gdn_conv_w])]
    win_p = _to_padded_cols(win_g[:, :, :cw].transpose(1, 0, 2).reshape(D, DPROJ))
    conv_full = conv_g.transpose(1, 0, 2).reshape(CONV_K, 3 * DGDN)

    loss_local, grad_x, (dwin_p, dwout, dwup3, dwdown), small = _local_step(
        x[0], loss_target[0], win_p, wout_g.reshape(D, D), wup3, wdown_g.reshape(DFF, D), row(pre_mix_norm), fox_f_bias,
        row(fox_out_norm), conv_full, gdn_a_log, gdn_dt_bias, row(gdn_out_norm), row(post_mix_norm), row(pre_mlp_norm),
        row(post_mlp_norm))
    loss = lax.psum(loss_local, ("x", "y", "c"))

    dwin3 = jnp.pad(_from_padded_cols(dwin_p).reshape(D, NCHIP, cw).transpose(1, 0, 2), ((0, 0), (0, 0), (0, D - cw)))
    blocks = [dwin3, dwout.reshape(NCHIP, D // NCHIP, D), dwup3, dwdown.reshape(NCHIP, DFF // NCHIP, D)]
    pair_sums = _pair_sum(blocks, _grads_pair_exchange(blocks))
    g_mine = _chip_sum(pair_sums, _grads_chip_exchange(pair_sums))
    g_theirs = _grads_pair_share(g_mine)

    g_small, d_small, nm_small, nv_small = _small_adamw(
        _small_allreduce(*small), [row(weights[n]) for n in SMALL_NAMES], [row(m_in[n]) for n in SMALL_NAMES],
        [row(v_in[n]) for n in SMALL_NAMES])

    g_big, d_big, nm_big, nv_big = _adamw_big([weights[n] for n in big], g_mine, g_theirs, [m_in[n] for n in big],
                                              [v_in[n] for n in big])

    grads, delta, new_m, new_v = {}, {}, {}, {}
    for i, n in enumerate(big):
        grads[n], delta[n], new_m[n], new_v[n] = g_big[i], d_big[i], nm_big[i], nv_big[i]
    for i, n in enumerate(SMALL_NAMES):
        shape = weights[n].shape
        grads[n], delta[n], new_m[n], new_v[n] = (g_small[i].reshape(shape), d_small[i].reshape(shape),
                                                  nm_small[i].reshape(shape), nv_small[i].reshape(shape))
    return (loss, grad_x[None], *[grads[n] for n in order_w], *[delta[n] for n in order_w], *[new_m[n] for n in order_w],
            *[new_v[n] for n in order_w])
```

```python
import jax
import jax.numpy as jnp
from jax import lax
from jax.experimental import pallas as pl
from jax.experimental.pallas import tpu as pltpu

F32 = jnp.float32
BF16 = jnp.bfloat16
MESH = pl.DeviceIdType.MESH

S = 2048
D = 1024
NFH, FHD = 8, 64
NPAIR = NFH // 2
NGH, GHD = 4, 128
DFOX = NFH * FHD
DGDN = NGH * GHD
CHUNK = 64
NCH = S // CHUNK
CONV_K = 4
DFF = 4 * D
EPS = 1e-6
DPROJ = 3600
LANES = 128
DPROJ_PAD = 3840
BLK_GDN = 12
BLK_GZ = 24
BLK_SMALL = 28
NCHIP = 4
NDEV = 8
VMEM_LIMIT = 56 * 1024 * 1024

ADAM_LR = 0.001
ADAM_B1 = 0.9
ADAM_B2 = 0.999
ADAM_EPS = 1e-08
ADAM_WD = 0.01
ADAM_STEP = 10


def _cparams(**kw):
    return pltpu.CompilerParams(vmem_limit_bytes=VMEM_LIMIT, **kw)


def _dn(ca, cb):
    return (((ca,), (cb,)), ((), ()))


def _dot(a, b, ca=1, cb=0):
    return lax.dot_general(a.astype(BF16), b.astype(BF16), _dn(ca, cb), preferred_element_type=F32)


def _hdot(a, b, ca=1, cb=0):
    return lax.dot_general(a.astype(F32), b.astype(F32), _dn(ca, cb), precision=lax.Precision.HIGHEST,
                           preferred_element_type=F32)


@jax.custom_vjp
def _mm_nn(a, b):
    return _dot(a, b, 1, 0)


def _mm_nn_fwd(a, b):
    return _dot(a, b, 1, 0), (a, b)


def _mm_nn_bwd(res, g):
    a, b = res
    return _dot(g, b, 1, 1), _dot(a, g, 0, 0)


_mm_nn.defvjp(_mm_nn_fwd, _mm_nn_bwd)


@jax.custom_vjp
def _mm_nt(a, b):
    return _dot(a, b, 1, 1)


def _mm_nt_fwd(a, b):
    return _dot(a, b, 1, 1), (a, b)


def _mm_nt_bwd(res, g):
    a, b = res
    return _dot(g, b, 1, 0), _dot(g, a, 0, 0)


_mm_nt.defvjp(_mm_nt_fwd, _mm_nt_bwd)


@jax.custom_vjp
def _saved_inverse(m, t_inv):
    del m
    return t_inv


def _saved_inverse_fwd(m, t_inv):
    del m
    return t_inv, t_inv


def _saved_inverse_bwd(t_inv, g):
    return -_hdot(_hdot(t_inv, g, 0, 0), t_inv, 1, 1), jnp.zeros_like(t_inv)


_saved_inverse.defvjp(_saved_inverse_fwd, _saved_inverse_bwd)


def _sigmoid(z):
    return 1.0 / (1.0 + jnp.exp(-z))


def _softplus(z):
    return jnp.maximum(z, 0.0) + jnp.log(1.0 + jnp.exp(-jnp.abs(z)))


def _silu(z):
    return z * _sigmoid(z)


def _rms_scale(x):
    return lax.rsqrt(jnp.mean(x * x, axis=-1, keepdims=True) + EPS)


def _rms_bwd(x, w, g):
    r = _rms_scale(x)
    gw = g * w
    dx = r * gw - x * (r * r * r) * jnp.mean(gw * x, axis=-1, keepdims=True)
    return dx, g * x * r


def _matmul(a, b, *, name, ta=False, tb=False, tm=512, tn=512, tk=512, out_dtypes=(F32,), b3=False, o3=False,
            extra=(), epilogue=None):
    m, k = (a.shape[1], a.shape[0]) if ta else a.shape
    if b3:
        n = b.shape[1] if tb else b.shape[0] * b.shape[2]
        kb = b.shape[0] * b.shape[2] if tb else b.shape[1]
    else:
        n, kb = (b.shape[0], b.shape[1]) if tb else (b.shape[1], b.shape[0])
    assert kb == k, (name, kb, k)
    tm, tn, tk = min(tm, m), min(tn, n), min(tk, k)
    assert m % tm == 0 and n % tn == 0 and k % tk == 0, (name, m, n, k, tm, tn, tk)
    nk = k // tk
    n_extra = len(extra)
    n_out = len(out_dtypes)

    def body(*refs):
        a_ref, b_ref = refs[0], refs[1]
        extra_refs = refs[2:2 + n_extra]
        out_refs = refs[2 + n_extra:2 + n_extra + n_out]

        def finish(acc):
            outs = (acc,) if epilogue is None else epilogue(acc, *[r[...] for r in extra_refs])
            for o_ref, val in zip(out_refs, outs):
                o_ref[...] = val.astype(o_ref.dtype)

        part = _dot(a_ref[...], b_ref[...], 0 if ta else 1, 1 if tb else 0)
        if nk == 1:
            finish(part)
            return
        acc_ref = refs[-1]
        kk = pl.program_id(2)

        @pl.when(kk == 0)
        def _():
            acc_ref[...] = part

        @pl.when(kk > 0)
        def _():
            acc_ref[...] += part

        @pl.when(kk == nk - 1)
        def _():
            finish(acc_ref[...])

    a_spec = pl.BlockSpec((tk, tm), lambda i, j, kk: (kk, i)) if ta else pl.BlockSpec((tm, tk), lambda i, j, kk: (i, kk))
    if b3 and tb:
        assert b.shape[2] == tk
        b_spec = pl.BlockSpec((None, tn, tk), lambda i, j, kk: (kk, j, 0))
    elif b3:
        assert b.shape[2] == tn
        b_spec = pl.BlockSpec((None, tk, tn), lambda i, j, kk: (j, kk, 0))
    elif tb:
        b_spec = pl.BlockSpec((tn, tk), lambda i, j, kk: (j, kk))
    else:
        b_spec = pl.BlockSpec((tk, tn), lambda i, j, kk: (kk, j))
    tile = pl.BlockSpec((tm, tn), lambda i, j, kk: (i, j))
    out_specs = [tile] * n_out
    out_shape = [jax.ShapeDtypeStruct((m, n), dt) for dt in out_dtypes]
    if o3:
        out_specs[0] = pl.BlockSpec((None, tm, tn), lambda i, j, kk: (j, i, 0))
        out_shape[0] = jax.ShapeDtypeStruct((n // tn, m, tn), out_dtypes[0])
    res = pl.pallas_call(
        body, name=name, grid=(m // tm, n // tn, nk),
        in_specs=[a_spec, b_spec] + [tile] * n_extra, out_specs=out_specs, out_shape=out_shape,
        scratch_shapes=[pltpu.VMEM((tm, tn), F32)] if nk > 1 else [],
        compiler_params=_cparams(dimension_semantics=("parallel", "parallel", "arbitrary")),
    )(a, b, *extra)
    return res[0] if n_out == 1 else res


TR = 256


def _row_spec(cols):
    return pl.BlockSpec((TR, cols), lambda i: (i, 0))


def _vec_spec(cols):
    return pl.BlockSpec((1, cols), lambda i: (0, 0))


def _pre_norm(x, w):
    def body(x_ref, w_ref, h_ref):
        xv = x_ref[...]
        h_ref[...] = (xv * _rms_scale(xv) * w_ref[...]).astype(BF16)

    return pl.pallas_call(
        body, name="pre_norm", grid=(S // TR,), in_specs=[_row_spec(D), _vec_spec(D)], out_specs=_row_spec(D),
        out_shape=jax.ShapeDtypeStruct((S, D), BF16), compiler_params=_cparams(),
    )(x, w)


def _post_mix(x, mixed, w_post, w_pre_mlp):
    def body(x_ref, m_ref, wp_ref, wm_ref, x1_ref, h2_ref):
        mv = m_ref[...]
        x1 = x_ref[...] + mv * _rms_scale(mv) * wp_ref[...]
        x1_ref[...] = x1
        h2_ref[...] = (x1 * _rms_scale(x1) * wm_ref[...]).astype(BF16)

    return pl.pallas_call(
        body, name="post_mix", grid=(S // TR,),
        in_specs=[_row_spec(D), _row_spec(D), _vec_spec(D), _vec_spec(D)], out_specs=[_row_spec(D), _row_spec(D)],
        out_shape=[jax.ShapeDtypeStruct((S, D), F32), jax.ShapeDtypeStruct((S, D), BF16)], compiler_params=_cparams(),
    )(x, mixed, w_post, w_pre_mlp)


def _loss_head(x1, y, w_post_mlp, target):
    def body(x1_ref, y_ref, w_ref, t_ref, dx2_ref, dy_ref, dw_ref, loss_ref):
        i = pl.program_id(0)
        yv = y_ref[...]
        w = w_ref[...]
        x2 = x1_ref[...] + yv * _rms_scale(yv) * w
        err = x2 - t_ref[...]
        dx2 = err * (1.0 / D)
        dx2_ref[...] = dx2
        dy, dwt = _rms_bwd(yv, w, dx2)
        dy_ref[...] = dy.astype(BF16)

        @pl.when(i == 0)
        def _():
            dw_ref[...] = jnp.zeros_like(dw_ref)
            loss_ref[...] = jnp.zeros_like(loss_ref)

        dw_ref[...] += jnp.sum(dwt, axis=0, keepdims=True)
        part = 0.5 * jnp.sum(jnp.mean(err * err, axis=-1, keepdims=True), axis=0, keepdims=True)
        loss_ref[...] += jnp.broadcast_to(part, loss_ref.shape)

    return pl.pallas_call(
        body, name="loss_head", grid=(S // TR,),
        in_specs=[_row_spec(D), _row_spec(D), _vec_spec(D), _row_spec(D)],
        out_specs=[_row_spec(D), _row_spec(D), _vec_spec(D), _vec_spec(LANES)],
        out_shape=[jax.ShapeDtypeStruct((S, D), F32), jax.ShapeDtypeStruct((S, D), BF16),
                   jax.ShapeDtypeStruct((1, D), F32), jax.ShapeDtypeStruct((1, LANES), F32)],
        compiler_params=_cparams(),
    )(x1, y, w_post_mlp, target)


def _mid_bwd(dh2, x1, w_pre_mlp, dx2, mixed, w_post):
    def body(dh2_ref, x1_ref, wm_ref, dx2_ref, m_ref, wp_ref, dx1_ref, dm_ref, dwm_ref, dwp_ref):
        i = pl.program_id(0)
        dxa, dwm = _rms_bwd(x1_ref[...], wm_ref[...], dh2_ref[...])
        dx1 = dx2_ref[...] + dxa
        dx1_ref[...] = dx1
        dm, dwp = _rms_bwd(m_ref[...], wp_ref[...], dx1)
        dm_ref[...] = dm.astype(BF16)

        @pl.when(i == 0)
        def _():
            dwm_ref[...] = jnp.zeros_like(dwm_ref)
            dwp_ref[...] = jnp.zeros_like(dwp_ref)

        dwm_ref[...] += jnp.sum(dwm, axis=0, keepdims=True)
        dwp_ref[...] += jnp.sum(dwp, axis=0, keepdims=True)

    return pl.pallas_call(
        body, name="mid_bwd", grid=(S // TR,),
        in_specs=[_row_spec(D), _row_spec(D), _vec_spec(D), _row_spec(D), _row_spec(D), _vec_spec(D)],
        out_specs=[_row_spec(D), _row_spec(D), _vec_spec(D), _vec_spec(D)],
        out_shape=[jax.ShapeDtypeStruct((S, D), F32), jax.ShapeDtypeStruct((S, D), BF16),
                   jax.ShapeDtypeStruct((1, D), F32), jax.ShapeDtypeStruct((1, D), F32)],
        compiler_params=_cparams(),
    )(dh2, x1, w_pre_mlp, dx2, mixed, w_post)


def _pre_norm_bwd(dh, x, w, dx1):
    def body(dh_ref, x_ref, w_ref, dx1_ref, dx_ref, dw_ref):
        i = pl.program_id(0)
        dxa, dwt = _rms_bwd(x_ref[...], w_ref[...], dh_ref[...])
        dx_ref[...] = dx1_ref[...] + dxa

        @pl.when(i == 0)
        def _():
            dw_ref[...] = jnp.zeros_like(dw_ref)

        dw_ref[...] += jnp.sum(dwt, axis=0, keepdims=True)

    return pl.pallas_call(
        body, name="pre_norm_bwd", grid=(S // TR,),
        in_specs=[_row_spec(D), _row_spec(D), _vec_spec(D), _row_spec(D)], out_specs=[_row_spec(D), _vec_spec(D)],
        out_shape=[jax.ShapeDtypeStruct((S, D), F32), jax.ShapeDtypeStruct((1, D), F32)], compiler_params=_cparams(),
    )(dh, x, w, dx1)


BQ = 256
NQ = S // BQ
LANE_BETA, LANE_G = 8, 12


def _gate_lanes(shape):
    lane = lax.broadcasted_iota(jnp.int32, shape, 1)
    return lane < LANE_BETA, (lane >= LANE_BETA) & (lane < LANE_G), (lane >= LANE_G) & (lane < LANE_G + NGH)


def _gates(proj, bias_vec, alog_vec):
    def body(s_ref, b_ref, a_ref, o_ref, t_ref, carry_ref):
        i = pl.program_id(0)

        @pl.when(i == 0)
        def _():
            carry_ref[...] = jnp.zeros_like(carry_ref)

        z = s_ref[...] + b_ref[...]
        tail = jnp.log(1.0 + jnp.exp(-jnp.abs(z)))
        sp = jnp.maximum(z, 0.0) + tail
        lf = jnp.minimum(z, 0.0) - tail
        r = lax.broadcasted_iota(jnp.int32, (BQ, BQ), 0)
        c = lax.broadcasted_iota(jnp.int32, (BQ, BQ), 1)
        tri = (c <= r).astype(F32)
        cum = _hdot(tri, lf) + carry_ref[...]
        carry_ref[...] = cum[BQ - 1:BQ, :]
        is_fox, is_beta, is_g = _gate_lanes(z.shape)
        out = jnp.where(is_fox, cum, jnp.where(is_beta, _sigmoid(z), jnp.where(is_g, -jnp.exp(a_ref[...]) * sp, 0.0)))
        o_ref[...] = out
        t_ref[...] = out.T

    return pl.pallas_call(
        body, name="gates", grid=(NQ,),
        in_specs=[pl.BlockSpec((BQ, LANES), lambda i: (i, BLK_SMALL)), _vec_spec(LANES), _vec_spec(LANES)],
        out_specs=[pl.BlockSpec((BQ, LANES), lambda i: (i, 0)), pl.BlockSpec((None, LANES, BQ), lambda i: (i, 0, 0))],
        out_shape=[jax.ShapeDtypeStruct((S, LANES), F32), jax.ShapeDtypeStruct((NQ, LANES, BQ), F32)],
        scratch_shapes=[pltpu.VMEM((1, LANES), F32)], compiler_params=_cparams(),
    )(proj, bias_vec, alog_vec)


def _gates_bwd(proj, bias_vec, alog_vec, dgates_gdn, dcum_fox, dproj):
    def body(s_ref, b_ref, a_ref, dg_ref, dc_ref, dproj_in, dproj_ref, red_ref, carry_ref):
        del dproj_in
        i = pl.program_id(0)

        @pl.when(i == 0)
        def _():
            carry_ref[...] = jnp.zeros_like(carry_ref)
            red_ref[...] = jnp.zeros_like(red_ref)

        z = s_ref[...] + b_ref[...]
        dg = dg_ref[...] + dc_ref[...]
        r = lax.broadcasted_iota(jnp.int32, (BQ, BQ), 0)
        c = lax.broadcasted_iota(jnp.int32, (BQ, BQ), 1)
        upper = (c >= r).astype(F32)
        dlf = _hdot(upper, dg) + carry_ref[...]
        carry_ref[...] = dlf[0:1, :]
        sig = _sigmoid(z)
        g_scale = -jnp.exp(a_ref[...])
        is_fox, is_beta, is_g = _gate_lanes(z.shape)
        ds = jnp.where(is_fox, dlf * (1.0 - sig), jnp.where(is_beta, dg * sig * (1.0 - sig), jnp.where(is_g, dg * g_scale * sig, 0.0)))
        dproj_ref[:, 0:LANES] = ds.astype(BF16)
        dproj_ref[:, LANES:2 * LANES] = jnp.zeros((BQ, LANES), BF16)
        dalog = jnp.where(is_g, dg * g_scale * _softplus(z), 0.0)
        sums = jnp.sum(ds, axis=0, keepdims=True)
        red_ref[0:1, :] += jnp.where(is_fox[0:1], sums, 0.0)
        red_ref[1:2, :] += pltpu.roll(jnp.where(is_g[0:1], sums, 0.0), LANES - LANE_G, 1)
        red_ref[2:3, :] += pltpu.roll(jnp.sum(dalog, axis=0, keepdims=True), LANES - LANE_G, 1)

    blk = pl.BlockSpec((BQ, LANES), lambda i: (NQ - 1 - i, 0))
    return pl.pallas_call(
        body, name="gates_bwd", grid=(NQ,),
        in_specs=[pl.BlockSpec((BQ, LANES), lambda i: (NQ - 1 - i, BLK_SMALL)), _vec_spec(LANES), _vec_spec(LANES), blk, blk,
                  pl.BlockSpec(memory_space=pl.ANY)],
        out_specs=[pl.BlockSpec((BQ, 2 * LANES), lambda i: (NQ - 1 - i, BLK_SMALL // 2)), pl.BlockSpec((8, LANES), lambda i: (0, 0))],
        out_shape=[jax.ShapeDtypeStruct((S, DPROJ_PAD), BF16), jax.ShapeDtypeStruct((8, LANES), F32)],
        input_output_aliases={5: 0},
        scratch_shapes=[pltpu.VMEM((1, LANES), F32)], compiler_params=_cparams(),
    )(proj, bias_vec, alog_vec, dgates_gdn, dcum_fox, dproj)


FOX_SCALE = FHD ** -0.5


def _head_mask(e):
    lane = lax.broadcasted_iota(jnp.int32, (1, LANES), 1)
    return (lane >= e * FHD) & (lane < (e + 1) * FHD)


def _lane_col(vals, index):
    lane = lax.broadcasted_iota(jnp.int32, vals.shape, 1)
    return jnp.sum(jnp.where(lane == index, vals, 0.0), axis=1, keepdims=True)


def _sublane_row(vals, index):
    row = lax.broadcasted_iota(jnp.int32, vals.shape, 0)
    return jnp.sum(jnp.where(row == index, vals, 0.0), axis=0, keepdims=True)


def _pair_cols(c0, c1):
    lane = lax.broadcasted_iota(jnp.int32, (c0.shape[0], 2), 1)
    return jnp.where(lane == 0, c0, c1)


def _fox_logits(q, k, cum_q, cum_k, qi, kj):
    s = _dot(q, k, 1, 1) * FOX_SCALE + (cum_q - cum_k)
    row = qi * BQ + lax.broadcasted_iota(jnp.int32, (BQ, BQ), 0)
    col = kj * BQ + lax.broadcasted_iota(jnp.int32, (BQ, BQ), 1)
    return jnp.where(col <= row, s, -jnp.inf)


def _head_rms(o, masks):
    o2 = o * o
    r = [lax.rsqrt(jnp.sum(jnp.where(mk, o2, 0.0), axis=1, keepdims=True) * (1.0 / FHD) + EPS) for mk in masks]
    return jnp.where(masks[0], r[0], r[1])


def _hosted(exchange):
    if exchange is None:
        return [], [], [], [], []
    return (exchange.inputs, [HBM] * len(exchange.inputs), [HBM] * len(exchange.out_shape), exchange.out_shape,
            exchange.sem_shapes())


def _fox_fwd(proj, gates, gates_t, w2, exchange=None):
    ex_in, ex_in_specs, ex_out_specs, ex_out_shape, ex_scratch = _hosted(exchange)

    def body(*refs):
        q_ref, k_ref, v_ref, g_ref, ct_ref, w_ref = refs[:6]
        mix_ref, o_ref, lse_ref = refs[6 + len(ex_in):9 + len(ex_in)]
        kb_ref, vb_ref = refs[9 + len(ex_in) + len(ex_out_shape):11 + len(ex_in) + len(ex_out_shape)]
        ex_refs = refs[6:6 + len(ex_in)] + refs[9 + len(ex_in):9 + len(ex_in) + len(ex_out_shape)] + refs[-2:]
        hp, qi = pl.program_id(0), pl.program_id(1)

        if exchange is not None:
            @pl.when((hp == 0) & (qi == 0))
            def _():
                exchange.start(*exchange.split(ex_refs))

        @pl.when(qi == 0)
        def _():
            kb_ref[...] = k_ref[...].astype(BF16)
            vb_ref[...] = v_ref[...].astype(BF16)

        masks = [_head_mask(0), _head_mask(1)]
        qblk = q_ref[...]
        gt = g_ref[...]
        qs = [jnp.where(mk, qblk, 0.0).astype(BF16) for mk in masks]
        cqs = [_lane_col(gt, 2 * hp + e) for e in range(2)]

        def step(kj, carry):
            rows = pl.ds(pl.multiple_of(kj * BQ, BQ), BQ)
            kb = kb_ref[rows, :]
            vb = vb_ref[rows, :]
            ctk = ct_ref[kj]
            new = []
            for e in range(2):
                m, l, acc = carry[e]
                s = _fox_logits(qs[e], kb, cqs[e], _sublane_row(ctk, 2 * hp + e), qi, kj)
                m_new = jnp.maximum(m, jnp.max(s, axis=-1, keepdims=True))
                p = jnp.exp(s - m_new)
                alpha = jnp.exp(m - m_new)
                new.append((m_new, alpha * l + jnp.sum(p, axis=-1, keepdims=True), alpha * acc + _dot(p, vb)))
            return tuple(new)

        one = (jnp.full((BQ, 1), -jnp.inf, F32), jnp.zeros((BQ, 1), F32), jnp.zeros((BQ, LANES), F32))
        (m0, l0, a0), (m1, l1, a1) = lax.fori_loop(0, qi + 1, step, (one, one))
        o = jnp.where(masks[0], a0 / l0, a1 / l1)
        o_ref[...] = o
        mix_ref[...] = (o * _head_rms(o, masks) * w_ref[...]).astype(BF16)
        lse_ref[...] = _pair_cols(m0 + jnp.log(l0), m1 + jnp.log(l1))

        if exchange is not None:
            @pl.when((hp == NPAIR - 1) & (qi == NQ - 1))
            def _():
                exchange.finish(*exchange.split(ex_refs))

    def col(part):
        return pl.BlockSpec((S, LANES), lambda hp, i: (0, 3 * hp + part))

    blk = pl.BlockSpec((BQ, LANES), lambda hp, i: (i, hp))
    res = pl.pallas_call(
        body, name="fox_fwd", grid=(NPAIR, NQ),
        in_specs=[pl.BlockSpec((BQ, LANES), lambda hp, i: (i, 3 * hp)), col(1), col(2),
                  pl.BlockSpec((BQ, LANES), lambda hp, i: (i, 0)), pl.BlockSpec((NQ, 8, BQ), lambda hp, i: (0, 0, 0)),
                  pl.BlockSpec((1, LANES), lambda hp, i: (0, 0))] + ex_in_specs,
        out_specs=[blk, blk, pl.BlockSpec((None, BQ, 2), lambda hp, i: (hp, i, 0))] + ex_out_specs,
        out_shape=[jax.ShapeDtypeStruct((S, D), BF16), jax.ShapeDtypeStruct((S, DFOX), F32),
                   jax.ShapeDtypeStruct((NPAIR, S, 2), F32)] + ex_out_shape,
        scratch_shapes=[pltpu.VMEM((S, LANES), BF16), pltpu.VMEM((S, LANES), BF16)] + ex_scratch,
        compiler_params=_cparams(),
    )(proj, proj, proj, gates, gates_t, w2, *ex_in)
    return res[0], res[1], res[2], res[3:]


def _fox_norm_bwd(o, dmix, w2):
    def body(o_ref, g_ref, w_ref, do_ref, dl_ref, dw_ref):
        hp, qi = pl.program_id(0), pl.program_id(1)
        masks = [_head_mask(0), _head_mask(1)]
        ov = o_ref[...]
        g = g_ref[...]
        r = _head_rms(ov, masks)
        gw = g * w_ref[...]
        gwo = gw * ov
        mean = [jnp.sum(jnp.where(mk, gwo, 0.0), axis=1, keepdims=True) * (1.0 / FHD) for mk in masks]
        do = r * gw - ov * (r * r * r) * jnp.where(masks[0], mean[0], mean[1])
        do_ref[...] = do.astype(BF16)
        doo = do * ov
        dl_ref[...] = _pair_cols(*[jnp.sum(jnp.where(mk, doo, 0.0), axis=1, keepdims=True) for mk in masks])

        @pl.when((hp == 0) & (qi == 0))
        def _():
            dw_ref[...] = jnp.zeros_like(dw_ref)

        dw_ref[...] += jnp.sum(g * ov * r, axis=0, keepdims=True)

        @pl.when((hp == NPAIR - 1) & (qi == NQ - 1))
        def _():
            dw = dw_ref[...]
            dw_ref[...] = dw + pltpu.roll(dw, FHD, 1)

    blk = pl.BlockSpec((BQ, LANES), lambda hp, i: (i, hp))
    vec = pl.BlockSpec((1, LANES), lambda hp, i: (0, 0))
    return pl.pallas_call(
        body, name="fox_norm_bwd", grid=(NPAIR, NQ), in_specs=[blk, blk, vec],
        out_specs=[blk, pl.BlockSpec((None, BQ, 2), lambda hp, i: (hp, i, 0)), vec],
        out_shape=[jax.ShapeDtypeStruct((S, DFOX), BF16), jax.ShapeDtypeStruct((NPAIR, S, 2), F32),
                   jax.ShapeDtypeStruct((1, LANES), F32)],
        compiler_params=_cparams(),
    )(o, dmix, w2)


def _fox_bwd(proj, do, gates, gates_t, lse, delta, exchange=None):
    ex_in, ex_in_specs, ex_out_specs, ex_out_shape, ex_scratch = _hosted(exchange)

    def body(*refs):
        q_ref, k_ref, v_ref, do_ref, g_ref, ct_ref, lse_ref, dl_ref = refs[:8]
        dproj_ref, dc_ref = refs[8 + len(ex_in):10 + len(ex_in)]
        qb_ref, dq_ref = refs[10 + len(ex_in) + len(ex_out_shape):12 + len(ex_in) + len(ex_out_shape)]
        ex_refs = refs[8:8 + len(ex_in)] + refs[10 + len(ex_in):10 + len(ex_in) + len(ex_out_shape)] + refs[-2:]
        hp, kj = pl.program_id(0), pl.program_id(1)

        if exchange is not None:
            @pl.when((hp == 0) & (kj == 0))
            def _():
                exchange.start(*exchange.split(ex_refs))

        @pl.when(kj == 0)
        def _():
            qb_ref[...] = q_ref[...].astype(BF16)
            dq_ref[...] = jnp.zeros_like(dq_ref)

        @pl.when((hp == 0) & (kj == 0))
        def _():
            dc_ref[...] = jnp.zeros_like(dc_ref)

        masks = [_head_mask(0), _head_mask(1)]
        kb = k_ref[...].astype(BF16)
        vb = v_ref[...].astype(BF16)
        ctk = ct_ref[...]
        cks = [_sublane_row(ctk, 2 * hp + e) for e in range(2)]
        lane = lax.broadcasted_iota(jnp.int32, (BQ, LANES), 1)

        def step(qi, carry):
            dk, dv, cs = carry
            rows = pl.ds(pl.multiple_of(qi * BQ, BQ), BQ)
            qv = qb_ref[rows, :]
            dov = do_ref[rows, :]
            gt = g_ref[rows, :]
            lse2 = lse_ref[rows, :]
            dl2 = dl_ref[rows, :]
            dq = jnp.zeros((BQ, LANES), F32)
            dc = jnp.zeros((BQ, LANES), F32)
            cs_new = []
            for e in range(2):
                h = 2 * hp + e
                qe = jnp.where(masks[e], qv, jnp.zeros_like(qv))
                doe = jnp.where(masks[e], dov, jnp.zeros_like(dov))
                s = _fox_logits(qe, kb, _lane_col(gt, h), cks[e], qi, kj)
                p = jnp.exp(s - _lane_col(lse2, e))
                dp = _dot(doe, vb, 1, 1)
                ds = p * (dp - _lane_col(dl2, e))
                dv = dv + _dot(p, doe, 0, 0)
                dk = dk + _dot(ds, qe, 0, 0) * FOX_SCALE
                dq = dq + jnp.where(masks[e], _dot(ds, kb), 0.0) * FOX_SCALE
                cs_new.append(cs[e] + jnp.sum(ds, axis=0, keepdims=True))
                dc = dc + jnp.where(lane == h, jnp.sum(ds, axis=1, keepdims=True), 0.0)
            dq_ref[rows, :] += dq
            dc_ref[rows, :] += dc
            return dk, dv, tuple(cs_new)

        zero = jnp.zeros((BQ, LANES), F32)
        dk, dv, cs = lax.fori_loop(kj, NQ, step, (zero, zero, (jnp.zeros((1, BQ), F32), jnp.zeros((1, BQ), F32))))
        krows = pl.ds(pl.multiple_of(kj * BQ, BQ), BQ)
        dproj_ref[krows, LANES:2 * LANES] = dk.astype(BF16)
        dproj_ref[krows, 2 * LANES:3 * LANES] = dv.astype(BF16)
        r = lax.broadcasted_iota(jnp.int32, (BQ, BQ), 0)
        c = lax.broadcasted_iota(jnp.int32, (BQ, BQ), 1)
        dcol = jnp.zeros((BQ, LANES), F32)
        for e in range(2):
            col = jnp.sum(jnp.where(r == c, cs[e], 0.0), axis=1, keepdims=True)
            dcol = dcol + jnp.where(lane == 2 * hp + e, col, 0.0)
        dc_ref[krows, :] -= dcol

        @pl.when(kj == NQ - 1)
        def _():
            dproj_ref[:, 0:LANES] = dq_ref[...].astype(BF16)

        if exchange is not None:
            @pl.when((hp == NPAIR - 1) & (kj == NQ - 1))
            def _():
                exchange.finish(*exchange.split(ex_refs))

    def blk(part):
        return pl.BlockSpec((BQ, LANES), lambda hp, j: (j, 3 * hp + part))

    pair = pl.BlockSpec((None, S, 2), lambda hp, j: (hp, 0, 0))
    res = pl.pallas_call(
        body, name="fox_bwd", grid=(NPAIR, NQ),
        in_specs=[pl.BlockSpec((S, LANES), lambda hp, j: (0, 3 * hp)), blk(1), blk(2),
                  pl.BlockSpec((S, LANES), lambda hp, j: (0, hp)), pl.BlockSpec((S, LANES), lambda hp, j: (0, 0)),
                  pl.BlockSpec((None, 8, BQ), lambda hp, j: (j, 0, 0)), pair, pair] + ex_in_specs,
        out_specs=[pl.BlockSpec((S, 3 * LANES), lambda hp, j: (0, hp)), pl.BlockSpec((S, LANES), lambda hp, j: (0, 0))]
        + ex_out_specs,
        out_shape=[jax.ShapeDtypeStruct((S, DPROJ_PAD), BF16), jax.ShapeDtypeStruct((S, LANES), F32)] + ex_out_shape,
        scratch_shapes=[pltpu.VMEM((S, LANES), BF16), pltpu.VMEM((S, LANES), F32)] + ex_scratch,
        compiler_params=_cparams(),
    )(proj, proj, proj, do, gates, gates_t, lse, delta, *ex_in)
    return res[0], res[1], res[2:]


NQKV = 3 * NGH
GDN_QSCALE = GHD ** -0.5


def _shift_down(x, s):
    if s == 0:
        return x
    row = lax.broadcasted_iota(jnp.int32, x.shape, 0)
    return jnp.where(row >= s, pltpu.roll(x, s, 0), 0.0)


def _shift_up(x, s):
    if s == 0:
        return x
    n = x.shape[0]
    row = lax.broadcasted_iota(jnp.int32, x.shape, 0)
    return jnp.where(row < n - s, pltpu.roll(x, n - s, 0), 0.0)


def _conv_pre(xv, wv):
    pre = xv * wv[CONV_K - 1:CONV_K, :]
    for j in range(CONV_K - 1):
        pre = pre + _shift_down(xv, CONV_K - 1 - j) * wv[j:j + 1, :]
    return pre


def _l2_factors(b):
    return b < 2 * NGH, jnp.where(b < NGH, GDN_QSCALE, 1.0)


def _gdn_pre(proj, conv_w):
    def body(x_ref, w_ref, o_ref):
        b = pl.program_id(0)
        c = _silu(_conv_pre(x_ref[...], w_ref[...]))
        normed, scale = _l2_factors(b)
        rs = lax.rsqrt(jnp.sum(c * c, axis=-1, keepdims=True) + EPS)
        o_ref[...] = c * jnp.where(normed, rs, 1.0) * scale

    return pl.pallas_call(
        body, name="gdn_pre", grid=(NQKV,),
        in_specs=[pl.BlockSpec((S, GHD), lambda b: (0, BLK_GDN + b)), pl.BlockSpec((CONV_K, GHD), lambda b: (0, b))],
        out_specs=pl.BlockSpec((S, GHD), lambda b: (0, b)),
        out_shape=jax.ShapeDtypeStruct((S, NQKV * GHD), F32), compiler_params=_cparams(),
    )(proj, conv_w)


def _gdn_pre_bwd(proj, conv_w, dqkv, dproj):
    def body(x_ref, w_ref, dy_ref, dproj_in, dx_ref, dw_ref):
        del dproj_in
        b = pl.program_id(0)
        xv = x_ref[...]
        wv = w_ref[...]
        pre = _conv_pre(xv, wv)
        sig = _sigmoid(pre)
        c = pre * sig
        normed, scale = _l2_factors(b)
        g = dy_ref[...] * scale
        rs = lax.rsqrt(jnp.sum(c * c, axis=-1, keepdims=True) + EPS)
        dc_n = rs * g - c * (rs * rs * rs) * jnp.sum(g * c, axis=-1, keepdims=True)
        dc = jnp.where(normed, dc_n, g)
        dpre = dc * sig * (1.0 + pre * (1.0 - sig))
        dx = dpre * wv[CONV_K - 1:CONV_K, :]
        for j in range(CONV_K - 1):
            dx = dx + _shift_up(dpre, CONV_K - 1 - j) * wv[j:j + 1, :]
        dx_ref[...] = dx.astype(BF16)
        for j in range(CONV_K):
            dw_ref[j:j + 1, :] = jnp.sum(dpre * _shift_down(xv, CONV_K - 1 - j), axis=0, keepdims=True)

    return pl.pallas_call(
        body, name="gdn_pre_bwd", grid=(NQKV,),
        in_specs=[pl.BlockSpec((S, GHD), lambda b: (0, BLK_GDN + b)), pl.BlockSpec((CONV_K, GHD), lambda b: (0, b)),
                  pl.BlockSpec((None, S, GHD), lambda b: (b // NGH, 0, b % NGH)), pl.BlockSpec(memory_space=pl.ANY)],
        out_specs=[pl.BlockSpec((S, GHD), lambda b: (0, BLK_GDN + b)), pl.BlockSpec((CONV_K, GHD), lambda b: (0, b))],
        out_shape=[jax.ShapeDtypeStruct((S, DPROJ_PAD), BF16), jax.ShapeDtypeStruct((CONV_K, NQKV * GHD), F32)],
        input_output_aliases={3: 0}, compiler_params=_cparams(),
    )(proj, conv_w, dqkv, dproj)


CB = 4
NCB = NCH // CB


def _chunk_prep(qs, ks, vs, gcols, bcols, t_saved=None):
    n = range(len(qs))
    r = lax.broadcasted_iota(jnp.int32, (CHUNK, CHUNK), 0)
    c = lax.broadcasted_iota(jnp.int32, (CHUNK, CHUNK), 1)
    incl = c <= r
    eye = (r == c).astype(F32)
    grow = [jnp.sum(gcols[i] * eye, axis=0, keepdims=True) for i in n]
    gc_col = [jnp.sum(jnp.where(incl, grow[i], 0.0), axis=1, keepdims=True) for i in n]
    gc_row = [jnp.sum(jnp.where(r <= c, gcols[i], 0.0), axis=0, keepdims=True) for i in n]
    decay = [jnp.exp(jnp.where(incl, gc_col[i] - gc_row[i], -jnp.inf)) for i in n]
    kb = [ks[i] * bcols[i] for i in n]
    vb = [vs[i] * bcols[i] for i in n]
    kk = [_mm_nt(kb[i], ks[i]) for i in n]
    m = [jnp.where(c < r, kk[i] * decay[i], 0.0) for i in n]
    if t_saved is None:
        t_inv = [eye - m[i] for i in n]
        p = [_hdot(m[i], m[i]) for i in n]
        for step in range(5):
            t_inv = [t_inv[i] + _hdot(t_inv[i], p[i]) for i in n]
            if step < 4:
                p = [_hdot(p[i], p[i]) for i in n]
    else:
        t_inv = [_saved_inverse(m[i], t_saved[i]) for i in n]
    egc = [jnp.exp(gc_col[i]) for i in n]
    u = [_mm_nn(t_inv[i], vb[i]) for i in n]
    w = [_mm_nn(t_inv[i], kb[i] * egc[i]) for i in n]
    qk = [_mm_nt(qs[i], ks[i]) for i in n]
    gc_last = [gc_col[i][CHUNK - 1:CHUNK, :] for i in n]
    return [(u[i], w[i], qk[i] * decay[i], qs[i] * egc[i], ks[i] * jnp.exp(gc_last[i] - gc_col[i]), jnp.exp(gc_last[i]),
             t_inv[i]) for i in n]


def _prep_specs():
    rows = CB * CHUNK
    qs = pl.BlockSpec((rows, GHD), lambda i, h: (i, h))
    ks = pl.BlockSpec((rows, GHD), lambda i, h: (i, NGH + h))
    vs = pl.BlockSpec((rows, GHD), lambda i, h: (i, 2 * NGH + h))
    gs = pl.BlockSpec((rows, LANES), lambda i, h: (i, 0))
    a_s = pl.BlockSpec((None, rows, CHUNK), lambda i, h: (h, i, 0))
    gl_s = pl.BlockSpec((None, CB, 1, LANES), lambda i, h: (h, i, 0, 0))
    return qs, ks, vs, gs, a_s, gl_s


def _gdn_prep(qkv, gates):
    def body(q_ref, k_ref, v_ref, g_ref, u_ref, w_ref, qd_ref, kd_ref, a_ref, gl_ref, t_ref):
        h = pl.program_id(1)
        chunks = [pl.ds(cidx * CHUNK, CHUNK) for cidx in range(CB)]
        gts = [g_ref[rows, :] for rows in chunks]
        outs = _chunk_prep([q_ref[rows, :] for rows in chunks], [k_ref[rows, :] for rows in chunks],
                           [v_ref[rows, :] for rows in chunks], [_lane_col(gt, LANE_G + h) for gt in gts],
                           [_lane_col(gt, LANE_BETA + h) for gt in gts])
        for cidx, rows in enumerate(chunks):
            u, w, a, qd, kd, gl, t_inv = outs[cidx]
            u_ref[rows, :] = u
            w_ref[rows, :] = w
            qd_ref[rows, :] = qd
            kd_ref[rows, :] = kd
            a_ref[rows, :] = a
            t_ref[rows, :] = t_inv
            gl_ref[cidx] = jnp.broadcast_to(gl, (1, LANES))

    qs, ks, vs, gs, a_s, gl_s = _prep_specs()
    tok = jax.ShapeDtypeStruct((S, DGDN), F32)
    sq = jax.ShapeDtypeStruct((NGH, S, CHUNK), F32)
    return pl.pallas_call(
        body, name="gdn_prep", grid=(NCB, NGH), in_specs=[qs, ks, vs, gs], out_specs=[qs, qs, qs, qs, a_s, gl_s, a_s],
        out_shape=[tok, tok, tok, tok, sq, jax.ShapeDtypeStruct((NGH, NCH, 1, LANES), F32), sq],
        compiler_params=_cparams(),
    )(qkv, qkv, qkv, gates)


def _gdn_prep_bwd(qkv, gates, t_inv, du, dw, dqd, dkd, da, dgl):
    def body(q_ref, k_ref, v_ref, g_ref, t_ref, du_ref, dw_ref, dqd_ref, dkd_ref, da_ref, dgl_ref, dqkv_ref, dg_ref):
        h = pl.program_id(1)

        @pl.when(h == 0)
        def _():
            dg_ref[...] = jnp.zeros_like(dg_ref)

        lane = lax.broadcasted_iota(jnp.int32, (CHUNK, LANES), 1)
        chunks = [pl.ds(cidx * CHUNK, CHUNK) for cidx in range(CB)]
        gts = [g_ref[rows, :] for rows in chunks]
        t_saved = [t_ref[rows, :] for rows in chunks]
        _, vjp = jax.vjp(lambda *args: [o[:6] for o in _chunk_prep(*args, t_saved=t_saved)],
                         [q_ref[rows, :] for rows in chunks], [k_ref[rows, :] for rows in chunks],
                         [v_ref[rows, :] for rows in chunks], [_lane_col(gt, LANE_G + h) for gt in gts],
                         [_lane_col(gt, LANE_BETA + h) for gt in gts])
        dqs, dks, dvs, dgcs, dbcs = vjp([(du_ref[rows, :], dw_ref[rows, :], da_ref[rows, :], dqd_ref[rows, :],
                                          dkd_ref[rows, :], dgl_ref[cidx][:, 0:1]) for cidx, rows in enumerate(chunks)])
        for cidx, rows in enumerate(chunks):
            dq, dk, dv, dgc, dbc = dqs[cidx], dks[cidx], dvs[cidx], dgcs[cidx], dbcs[cidx]
            dqkv_ref[0, rows, :] = dq
            dqkv_ref[1, rows, :] = dk
            dqkv_ref[2, rows, :] = dv
            dg_ref[rows, :] += jnp.where(lane == LANE_G + h, dgc, 0.0) + jnp.where(lane == LANE_BETA + h, dbc, 0.0)

    qs, ks, vs, gs, a_s, gl_s = _prep_specs()
    return pl.pallas_call(
        body, name="gdn_prep_bwd", grid=(NCB, NGH), in_specs=[qs, ks, vs, gs, a_s, qs, qs, qs, qs, a_s, gl_s],
        out_specs=[pl.BlockSpec((3, CB * CHUNK, GHD), lambda i, h: (0, i, h)), gs],
        out_shape=[jax.ShapeDtypeStruct((3, S, DGDN), F32), jax.ShapeDtypeStruct((S, LANES), F32)],
        compiler_params=_cparams(),
    )(qkv, qkv, qkv, gates, t_inv, du, dw, dqd, dkd, da, dgl)


def _scan_specs():
    hs = pl.BlockSpec((S, GHD), lambda h: (0, h))
    a_s = pl.BlockSpec((None, S, CHUNK), lambda h: (h, 0, 0))
    gl_s = pl.BlockSpec((None, NCH, 1, LANES), lambda h: (h, 0, 0, 0))
    st_s = pl.BlockSpec((None, NCH, GHD, GHD), lambda h: (h, 0, 0, 0))
    gz_s = pl.BlockSpec((S, GHD), lambda h: (0, BLK_GZ + h))
    mix_s = pl.BlockSpec((S, GHD), lambda h: (0, NPAIR + h))
    return hs, a_s, gl_s, st_s, gz_s, mix_s


def _gdn_scan(u, w, qd, kd, a, gl, proj, w_norm, mix):
    def body(u_ref, w_ref, qd_ref, kd_ref, a_ref, gl_ref, z_ref, wn_ref, mix_in, mix_ref, o_ref, st_ref):
        del mix_in

        def step(ci, state):
            rows = pl.ds(pl.multiple_of(ci * CHUNK, CHUNK), CHUNK)
            st_ref[ci] = state
            vn = u_ref[rows, :] - _dot(w_ref[rows, :], state)
            o_ref[rows, :] = _dot(qd_ref[rows, :], state) + _dot(a_ref[rows, :], vn)
            return state * gl_ref[ci] + _dot(kd_ref[rows, :], vn, 0, 0)

        lax.fori_loop(0, NCH, step, jnp.zeros((GHD, GHD), F32))
        ov = o_ref[...]
        mix_ref[...] = (ov * _rms_scale(ov) * wn_ref[...] * _silu(z_ref[...])).astype(BF16)

    hs, a_s, gl_s, st_s, gz_s, mix_s = _scan_specs()
    return pl.pallas_call(
        body, name="gdn_scan", grid=(NGH,),
        in_specs=[hs, hs, hs, hs, a_s, gl_s, gz_s, pl.BlockSpec((1, GHD), lambda h: (0, 0)), pl.BlockSpec(memory_space=pl.ANY)],
        out_specs=[mix_s, hs, st_s],
        out_shape=[jax.ShapeDtypeStruct((S, D), BF16), jax.ShapeDtypeStruct((S, DGDN), F32),
                   jax.ShapeDtypeStruct((NGH, NCH, GHD, GHD), F32)],
        input_output_aliases={8: 0}, compiler_params=_cparams(),
    )(u, w, qd, kd, a, gl, proj, w_norm, mix)


def _gdn_scan_bwd(dmix, o, proj, w_norm, u, w, qd, kd, a, gl, states, dproj):
    def body(dy_ref, o_ref, z_ref, wn_ref, u_ref, w_ref, qd_ref, kd_ref, a_ref, gl_ref, st_ref, dproj_in,
             dz_ref, du_ref, dw_ref, dqd_ref, dkd_ref, da_ref, dgl_ref, dwn_ref, do_ref):
        del dproj_in
        h = pl.program_id(0)
        ov = o_ref[...]
        zv = z_ref[...]
        wn = wn_ref[...]
        g = dy_ref[...]
        sig = _sigmoid(zv)
        dz_ref[...] = (g * (ov * _rms_scale(ov) * wn) * sig * (1.0 + zv * (1.0 - sig))).astype(BF16)
        do, dwt = _rms_bwd(ov, wn, g * zv * sig)
        do_ref[...] = do

        @pl.when(h == 0)
        def _():
            dwn_ref[...] = jnp.zeros_like(dwn_ref)

        dwn_ref[...] += jnp.sum(dwt, axis=0, keepdims=True)

        def step(t, dstate):
            ci = NCH - 1 - t
            rows = pl.ds(pl.multiple_of(ci * CHUNK, CHUNK), CHUNK)
            state = st_ref[ci]
            dov = do_ref[rows, :]
            wv = w_ref[rows, :]
            kdv = kd_ref[rows, :]
            vn = u_ref[rows, :] - _dot(wv, state)
            dvn = _dot(a_ref[rows, :], dov, 0, 0) + _dot(kdv, dstate)
            da_ref[rows, :] = _dot(dov, vn, 1, 1)
            dqd_ref[rows, :] = _dot(dov, state, 1, 1)
            dkd_ref[rows, :] = _dot(vn, dstate, 1, 1)
            dgl = jnp.sum(jnp.sum(dstate * state, axis=1, keepdims=True), axis=0, keepdims=True)
            dgl_ref[ci] = jnp.broadcast_to(dgl, (1, LANES))
            du_ref[rows, :] = dvn
            dw_ref[rows, :] = -_dot(dvn, state, 1, 1)
            return dstate * gl_ref[ci] + _dot(qd_ref[rows, :], dov, 0, 0) - _dot(wv, dvn, 0, 0)

        lax.fori_loop(0, NCH, step, jnp.zeros((GHD, GHD), F32))

    hs, a_s, gl_s, st_s, gz_s, mix_s = _scan_specs()
    vec = pl.BlockSpec((1, GHD), lambda h: (0, 0))
    tok = jax.ShapeDtypeStruct((S, DGDN), F32)
    return pl.pallas_call(
        body, name="gdn_scan_bwd", grid=(NGH,),
        in_specs=[mix_s, hs, gz_s, vec, hs, hs, hs, hs, a_s, gl_s, st_s, pl.BlockSpec(memory_space=pl.ANY)],
        out_specs=[gz_s, hs, hs, hs, hs, a_s, gl_s, vec],
        out_shape=[jax.ShapeDtypeStruct((S, DPROJ_PAD), BF16), tok, tok, tok, tok,
                   jax.ShapeDtypeStruct((NGH, S, CHUNK), F32), jax.ShapeDtypeStruct((NGH, NCH, 1, LANES), F32),
                   jax.ShapeDtypeStruct((1, GHD), F32)],
        input_output_aliases={11: 0}, scratch_shapes=[pltpu.VMEM((S, GHD), F32)], compiler_params=_cparams(),
    )(dmix, o, proj, w_norm, u, w, qd, kd, a, gl, states, dproj)


def _place():
    return lax.axis_index("x"), lax.axis_index("y"), lax.axis_index("c")


def _other_chips(x, y):
    return [(1 - x, y), (x, 1 - y), (1 - x, 1 - y)]


HBM = pl.BlockSpec(memory_space=pltpu.HBM)
VMEM = pl.BlockSpec(memory_space=pltpu.VMEM)


def _half_rows(ref_or_rows, half):
    rows = ref_or_rows // 2
    return pl.ds(pl.multiple_of(half * rows, rows), rows)


class _Exchange:
    def __init__(self, inputs, out_shape, n_sems, start, finish):
        self.inputs, self.out_shape, self.n_sems, self.start, self.finish = inputs, out_shape, n_sems, start, finish

    def sem_shapes(self):
        return [pltpu.SemaphoreType.DMA((self.n_sems,)), pltpu.SemaphoreType.DMA((self.n_sems,))]

    def split(self, refs):
        n_in, n_out = len(self.inputs), len(self.out_shape)
        return refs[:n_in], refs[n_in:n_in + n_out], refs[n_in + n_out], refs[n_in + n_out + 1]


def _run_exchange(ex, name):
    def body(*refs):
        parts = ex.split(refs)
        ex.start(*parts)
        ex.finish(*parts)

    return pl.pallas_call(
        body, name=name, in_specs=[HBM] * len(ex.inputs), out_specs=[HBM] * len(ex.out_shape), out_shape=ex.out_shape,
        scratch_shapes=ex.sem_shapes(), compiler_params=_cparams(),
    )(*ex.inputs)


def _allgather_exchange(shards, whole=()):
    n, nw = len(shards), len(whole)

    def plan(src, outs, send_sems, recv_sems):
        x, y, c = _place()
        chips = _other_chips(x, y)
        chip_ids = [2 * ch[0] + ch[1] for ch in chips]

        def copy(a, k, chip_index, half, to, from_src):
            rows = _half_rows(src[a].shape[0], half)
            dst = outs[a].at[chip_index, rows]
            return pltpu.make_async_remote_copy(
                src_ref=src[a].at[rows] if from_src else dst, dst_ref=dst, send_sem=send_sems.at[6 * a + k],
                recv_sem=recv_sems.at[6 * a + k], device_id=to, device_id_type=MESH)

        def whole_copy(b, k, chip_index, to):
            return pltpu.make_async_remote_copy(
                src_ref=src[n + b], dst_ref=outs[n + b].at[chip_index], send_sem=send_sems.at[6 * n + 3 * b + k],
                recv_sem=recv_sems.at[6 * n + 3 * b + k], device_id=to, device_id_type=MESH)

        me, sibling = (x, y, c), (x, y, 1 - c)
        first = [copy(a, j, 2 * x + y, c, (*chips[j], c), True) for a in range(n) for j in range(3)]
        first += [whole_copy(b, j, 2 * x + y, (*chips[j], c)) for b in range(nw) for j in range(3)]
        landing = [copy(a, j, chip_ids[j], c, me, False) for a in range(n) for j in range(3)]
        passed = [copy(a, 3 + j, chip_ids[j], c, sibling, False) for a in range(n) for j in range(3)]
        arriving = [copy(a, 3 + j, chip_ids[j], 1 - c, me, False) for a in range(n) for j in range(3)]
        arriving += [whole_copy(b, j, chip_ids[j], me) for b in range(nw) for j in range(3)]
        return first, landing, passed, arriving

    def start(*refs):
        for cp in plan(*refs)[0]:
            cp.start()

    def finish(*refs):
        first, landing, passed, arriving = plan(*refs)
        for lands, onward in zip(landing, passed):
            lands.wait_recv()
            onward.start()
        for cp in arriving:
            cp.wait_recv()
        for cp in first + passed:
            cp.wait_send()

    out_shape = [jax.ShapeDtypeStruct((NCHIP,) + s.shape, s.dtype) for s in list(shards) + list(whole)]
    return _Exchange(list(shards) + list(whole), out_shape, 6 * n + 3 * nw, start, finish)


def _with_own(gathered, own):
    x, y, _ = _place()
    return lax.dynamic_update_index_in_dim(gathered, own, 2 * x + y, axis=0)


def _simple_exchange(inputs, out_shape, copies_of):
    def start(*refs):
        for cp in copies_of(*refs):
            cp.start()

    def finish(*refs):
        for cp in copies_of(*refs):
            cp.wait()

    return _Exchange(list(inputs), out_shape, len(out_shape) * 3, start, finish)


def _pair_exchange(grads):
    def copies_of(src, outs, send_sems, recv_sems):
        x, y, c = _place()
        return [pltpu.make_async_remote_copy(
            src_ref=src[a].at[:, _half_rows(src[a].shape[1], 1 - c)], dst_ref=outs[a], send_sem=send_sems.at[a],
            recv_sem=recv_sems.at[a], device_id=(x, y, 1 - c), device_id_type=MESH) for a in range(len(src))]

    return _simple_exchange(
        grads, [jax.ShapeDtypeStruct((g.shape[0], g.shape[1] // 2, g.shape[2]), g.dtype) for g in grads], copies_of)


def _pair_sum(grads, theirs, name):
    n = len(grads)

    def body(*refs):
        south = lax.axis_index("c") == 0
        for a in range(n):
            g = refs[a][...]
            half = g.shape[0] // 2
            mine = jnp.where(south, g[:half], g[half:])
            refs[2 * n + a][...] = (mine.astype(F32) + refs[n + a][...].astype(F32)).astype(BF16)

    def specs(arrs):
        return [pl.BlockSpec((None,) + g.shape[1:], lambda j: (j, 0, 0)) for g in arrs]

    return pl.pallas_call(
        body, name=name, grid=(NCHIP,), in_specs=specs(grads) + specs(theirs), out_specs=specs(theirs),
        out_shape=[jax.ShapeDtypeStruct(g.shape, BF16) for g in theirs], compiler_params=_cparams(),
    )(*grads, *theirs)


def _chip_exchange(parts):
    def copies_of(src, outs, send_sems, recv_sems):
        x, y, c = _place()
        return [pltpu.make_async_remote_copy(
            src_ref=src[a].at[2 * chip[0] + chip[1]], dst_ref=outs[a].at[k], send_sem=send_sems.at[3 * a + k],
            recv_sem=recv_sems.at[3 * a + k], device_id=(*chip, c), device_id_type=MESH)
            for a in range(len(src)) for k, chip in enumerate(_other_chips(x, y))]

    return _simple_exchange(parts, [jax.ShapeDtypeStruct((NCHIP - 1,) + p.shape[1:], p.dtype) for p in parts], copies_of)


def _chip_sum(parts, received):
    n = len(parts)
    steps = 4

    def body(*refs):
        chip = 2 * lax.axis_index("x") + lax.axis_index("y")
        for a in range(n):
            p, r = refs[a], refs[n + a]
            own = jnp.where(chip == 0, p[0], jnp.where(chip == 1, p[1], jnp.where(chip == 2, p[2], p[3])))
            refs[2 * n + a][...] = ((own.astype(F32) + r[0].astype(F32)) + r[1].astype(F32)) + r[2].astype(F32)

    def specs(arrs):
        return [pl.BlockSpec((g.shape[0], g.shape[1] // steps, g.shape[2]), lambda i: (0, i, 0)) for g in arrs]

    out_specs = [pl.BlockSpec((g.shape[1] // steps, g.shape[2]), lambda i: (i, 0)) for g in parts]
    return pl.pallas_call(
        body, name="grads_chip_sum", grid=(steps,), in_specs=specs(parts) + specs(received), out_specs=out_specs,
        out_shape=[jax.ShapeDtypeStruct(g.shape[1:], F32) for g in parts], compiler_params=_cparams(),
    )(*parts, *received)


def _pair_share(halves):
    def copies_of(src, outs, send_sems, recv_sems):
        x, y, c = _place()
        return [pltpu.make_async_remote_copy(
            src_ref=src[a], dst_ref=outs[a], send_sem=send_sems.at[a], recv_sem=recv_sems.at[a],
            device_id=(x, y, 1 - c), device_id_type=MESH) for a in range(len(src))]

    return _simple_exchange(halves, [jax.ShapeDtypeStruct(h.shape, F32) for h in halves], copies_of)


def _adamw_math(w, g, m, v):
    nm = ADAM_B1 * m + (1.0 - ADAM_B1) * g
    nv = ADAM_B2 * v + (1.0 - ADAM_B2) * jnp.square(g)
    m_hat = nm / (1.0 - ADAM_B1 ** ADAM_STEP)
    v_hat = nv / (1.0 - ADAM_B2 ** ADAM_STEP)
    return -ADAM_LR * (m_hat / (jnp.sqrt(v_hat) + ADAM_EPS) + ADAM_WD * w), nm, nv


def _adamw_big(ws, g_mine, g_theirs, ms, vs):
    n = len(ws)
    steps = 8

    def body(*refs):
        own_half = (pl.program_id(0) // (steps // 2)) == lax.axis_index("c")
        for a in range(n):
            w = refs[a][...]
            g = jnp.where(own_half, refs[n + a][...], refs[2 * n + a][...])[:, :w.shape[1]]
            d, nm, nv = _adamw_math(w, g, refs[3 * n + a][...], refs[4 * n + a][...])
            refs[5 * n + a][...] = g
            refs[6 * n + a][...] = d
            refs[7 * n + a][...] = nm
            refs[8 * n + a][...] = nv

    specs = [pl.BlockSpec((w.shape[0] // steps, w.shape[1]), lambda i: (i, 0)) for w in ws]
    half_specs = [pl.BlockSpec((g.shape[0] // (steps // 2), g.shape[1]), lambda i: (i % (steps // 2), 0)) for g in g_mine]
    shapes = [jax.ShapeDtypeStruct(w.shape, F32) for w in ws]
    res = pl.pallas_call(
        body, name="adamw_big", grid=(steps,), in_specs=specs + half_specs * 2 + specs * 2, out_specs=specs * 4,
        out_shape=shapes * 4, compiler_params=_cparams(),
    )(*ws, *g_mine, *g_theirs, *ms, *vs)
    return res[:n], res[n:2 * n], res[2 * n:3 * n], res[3 * n:]


NORM_NAMES = ("pre_mix_norm", "post_mix_norm", "pre_mlp_norm", "post_mlp_norm")
SMALL_NAMES = NORM_NAMES + ("gdn_conv_w", "fox_f_bias", "gdn_dt_bias", "gdn_a_log", "fox_out_norm", "gdn_out_norm")
CONV_COLS = 3 * DGDN // NCHIP


def _small_allreduce(d_norms, d_conv, sums, d_fox_norm, d_gdn_norm):
    def body(*refs):
        dn_ref, dconv_ref, sums_ref, dfn_ref, dgn_ref = refs[:5]
        outs = refs[5:10]
        g_norms, g_conv, g_sums, g_fn, g_gn, send_sems, recv_sems, local_sem = refs[10:]
        x, y, c = _place()
        me = 4 * x + 2 * y + c
        g_norms[me] = dn_ref[...]
        g_sums[me] = sums_ref[...]
        g_fn[me] = dfn_ref[...]
        g_gn[me] = dgn_ref[...]

        def conv_cols(chip_index):
            return dconv_ref.at[:, pl.ds(pl.multiple_of(chip_index * CONV_COLS, LANES), CONV_COLS)]

        own = pltpu.make_async_copy(conv_cols(2 * x + y), g_conv.at[me], local_sem)
        own.start()
        copies = []
        for k in range(1, NDEV):
            px, py, pc = x ^ ((k >> 2) & 1), y ^ ((k >> 1) & 1), c ^ (k & 1)
            pairs = [(dn_ref, g_norms), (conv_cols(2 * px + py), g_conv), (sums_ref, g_sums), (dfn_ref, g_fn), (dgn_ref, g_gn)]
            for a, (src, dst) in enumerate(pairs):
                cp = pltpu.make_async_remote_copy(
                    src_ref=src, dst_ref=dst.at[me], send_sem=send_sems.at[5 * (k - 1) + a],
                    recv_sem=recv_sems.at[5 * (k - 1) + a], device_id=(px, py, pc), device_id_type=MESH)
                cp.start()
                copies.append(cp)
        own.wait()
        for cp in copies:
            cp.wait()

        def total(buf):
            acc = buf[0]
            for i in range(1, NDEV):
                acc = acc + buf[i]
            return acc

        for out, buf in zip(outs, (g_norms, g_conv, g_sums, g_fn, g_gn)):
            out[...] = total(buf)

    n_sem = 5 * (NDEV - 1)
    shapes = [(4, D), (CONV_K, CONV_COLS), (8, LANES), (1, LANES), (1, LANES)]
    return pl.pallas_call(
        body, name="small_allreduce", in_specs=[VMEM] * 5, out_specs=[VMEM] * 5,
        out_shape=[jax.ShapeDtypeStruct(s, F32) for s in shapes],
        scratch_shapes=[pltpu.VMEM((NDEV,) + s, F32) for s in shapes]
        + [pltpu.SemaphoreType.DMA((n_sem,)), pltpu.SemaphoreType.DMA((n_sem,)), pltpu.SemaphoreType.DMA],
        compiler_params=_cparams(),
    )(d_norms, d_conv, sums, d_fox_norm, d_gdn_norm)


def _small_adamw(totals, ws, ms, vs):
    n = len(SMALL_NAMES)

    def body(*refs):
        t_norms, t_conv, t_sums, t_fn, t_gn = [r[...] for r in refs[:5]]
        w_refs, m_refs, v_refs = refs[5:5 + n], refs[5 + n:5 + 2 * n], refs[5 + 2 * n:5 + 3 * n]
        outs = refs[5 + 3 * n:]
        grads = [t_norms[i:i + 1, :] for i in range(4)] + [
            t_conv, t_sums[0:1, 0:NFH], t_sums[1:2, 0:NGH], t_sums[2:3, 0:NGH], t_fn[:, 0:FHD], t_gn]
        for a in range(n):
            d, nm, nv = _adamw_math(w_refs[a][...], grads[a], m_refs[a][...], v_refs[a][...])
            outs[a][...] = grads[a]
            outs[n + a][...] = d
            outs[2 * n + a][...] = nm
            outs[3 * n + a][...] = nv

    def whole(arr):
        return pl.BlockSpec(arr.shape, lambda i: (0, 0))

    res = pl.pallas_call(
        body, name="small_adamw", grid=(1,), in_specs=[whole(t) for t in totals] + [whole(w) for w in ws] * 3,
        out_specs=[whole(w) for w in ws] * 4, out_shape=[jax.ShapeDtypeStruct(w.shape, F32) for w in ws] * 4,
        compiler_params=_cparams(),
    )(*totals, *ws, *ms, *vs)
    return res[:n], res[n:2 * n], res[2 * n:3 * n], res[3 * n:]


def _to_padded_cols(w):
    pieces = [w[:, part * DFOX + hp * LANES:part * DFOX + (hp + 1) * LANES] for hp in range(NPAIR) for part in range(3)]
    pieces += [w[:, 1544:3080], w[:, 3088:3600], w[:, 1536:1544], w[:, 3080:3088],
               jnp.zeros((w.shape[0], 2 * LANES - 16), w.dtype)]
    return jnp.concatenate(pieces, axis=1)


def _from_padded_cols(w):
    c0 = BLK_SMALL * LANES
    fox = [w[:, (3 * hp + part) * LANES:(3 * hp + part + 1) * LANES] for part in range(3) for hp in range(NPAIR)]
    return jnp.concatenate(fox + [w[:, c0:c0 + 8], w[:, BLK_GDN * LANES:BLK_GZ * LANES], w[:, c0 + 8:c0 + 16],
                                  w[:, BLK_GZ * LANES:BLK_SMALL * LANES]], axis=1)


def _local_step(x, target, win_p, late_weights, reduce_mlp, pre_mix_norm, fox_f_bias, fox_out_norm, conv_w, gdn_a_log,
                gdn_dt_bias, gdn_out_norm, post_mix_norm, pre_mlp_norm, post_mlp_norm):
    bias_vec = jnp.zeros((1, LANES), F32).at[0, 0:NFH].set(fox_f_bias).at[0, LANE_G:LANE_G + NGH].set(gdn_dt_bias)
    alog_vec = jnp.zeros((1, LANES), F32).at[0, LANE_G:LANE_G + NGH].set(gdn_a_log)
    w2 = jnp.concatenate([fox_out_norm, fox_out_norm], axis=1)

    h = _pre_norm(x, pre_mix_norm)
    proj = _matmul(h, win_p, tm=2048, tn=768, tk=1024, name="mm_proj")
    gates, gates_t = _gates(proj, bias_vec, alog_vec)
    mix, fox_o, lse, late = _fox_fwd(proj, gates, gates_t, w2, exchange=late_weights[0])
    wout, wup3, wdown = late_weights[1](late)
    qkv = _gdn_pre(proj, conv_w)
    u, w, qd, kd, a_intra, gl, t_inv = _gdn_prep(qkv, gates)
    mix, gdn_raw, states = _gdn_scan(u, w, qd, kd, a_intra, gl, proj, gdn_out_norm, mix)
    mixed = _matmul(mix, wout, tm=2048, tk=1024, name="mm_out")
    x1, h2 = _post_mix(x, mixed, post_mix_norm, pre_mlp_norm)

    def relu2(acc):
        r = jnp.maximum(acc, 0.0)
        return acc, r * r

    up, act = _matmul(h2, wup3, b3=True, tm=1024, tn=1024, tk=1024, out_dtypes=(F32, BF16), epilogue=relu2, name="mm_up")
    y = _matmul(act, wdown, tm=2048, tk=1024, name="mm_down")
    dx2, dy, d_post_mlp, loss_row = _loss_head(x1, y, post_mlp_norm, target)

    dwdown = _matmul(act, dy, ta=True, tm=1024, tn=1024, tk=2048, out_dtypes=(BF16,), name="mm_dwdown")

    def relu2_bwd(acc, upv):
        return (acc * 2.0 * jnp.maximum(upv, 0.0),)

    dup = _matmul(dy, wdown, tb=True, tm=1024, tn=1024, tk=1024, out_dtypes=(BF16,), extra=(up,), epilogue=relu2_bwd,
                  name="mm_dact")
    dwup3 = _matmul(h2, dup, ta=True, tm=1024, tn=1024, tk=2048, out_dtypes=(BF16,), o3=True, name="mm_dwup")
    dh2 = _matmul(dup, wup3, tb=True, b3=True, tm=2048, tk=1024, name="mm_dh2")
    dx1, dmixed, d_pre_mlp, d_post_mix = _mid_bwd(dh2, x1, pre_mlp_norm, dx2, mixed, post_mix_norm)
    dwout = _matmul(mix, dmixed, ta=True, tm=1024, tn=1024, tk=2048, out_dtypes=(BF16,), name="mm_dwout")
    dmix = _matmul(dmixed, wout, tb=True, tm=2048, tk=1024, name="mm_dmix")

    dfox, delta, d_fox_norm = _fox_norm_bwd(fox_o, dmix, w2)
    dproj, dcum_fox, reduced_mlp = _fox_bwd(proj, dfox, gates, gates_t, lse, delta, exchange=reduce_mlp(dwup3, dwdown))
    dproj, du, dw, dqd, dkd, da, dgl, d_gdn_norm = _gdn_scan_bwd(dmix, gdn_raw, proj, gdn_out_norm, u, w, qd, kd,
                                                                 a_intra, gl, states, dproj)
    dqkv, dgates_gdn = _gdn_prep_bwd(qkv, gates, t_inv, du, dw, dqd, dkd, da, dgl)
    dproj, d_conv = _gdn_pre_bwd(proj, conv_w, dqkv, dproj)
    dproj, sums = _gates_bwd(proj, bias_vec, alog_vec, dgates_gdn, dcum_fox, dproj)

    dwin_p = _matmul(h, dproj, ta=True, tm=1024, tn=1280, tk=2048, out_dtypes=(BF16,), name="mm_dwin")
    dh = _matmul(dproj, win_p, tb=True, tm=2048, tk=1280, name="mm_dh")
    grad_x, d_pre_mix = _pre_norm_bwd(dh, x, pre_mix_norm, dx1)

    d_norms = jnp.concatenate([d_pre_mix, d_post_mix, d_pre_mlp, d_post_mlp], axis=0)
    return (loss_row[0, 0], grad_x, (dwin_p, dwout, dwup3, dwdown), (d_norms, d_conv, sums, d_fox_norm, d_gdn_norm),
            reduced_mlp)


def kernel(x, pre_mix_norm, w_in, fox_f_bias, fox_out_norm, gdn_conv_w, gdn_a_log, gdn_dt_bias, gdn_out_norm, w_out, post_mix_norm, pre_mlp_norm, w_up, w_down, post_mlp_norm, loss_target, m_pre_mix_norm, m_w_in, m_fox_f_bias, m_fox_out_norm, m_gdn_conv_w, m_gdn_a_log, m_gdn_dt_bias, m_gdn_out_norm, m_w_out, m_post_mix_norm, m_pre_mlp_norm, m_w_up, m_w_down, m_post_mlp_norm, v_pre_mix_norm, v_w_in, v_fox_f_bias, v_fox_out_norm, v_gdn_conv_w, v_gdn_a_log, v_gdn_dt_bias, v_gdn_out_norm, v_w_out, v_post_mix_norm, v_pre_mlp_norm, v_w_up, v_w_down, v_post_mlp_norm):
    weights = dict(pre_mix_norm=pre_mix_norm, w_in=w_in, fox_f_bias=fox_f_bias, fox_out_norm=fox_out_norm, gdn_conv_w=gdn_conv_w,
                   gdn_a_log=gdn_a_log, gdn_dt_bias=gdn_dt_bias, gdn_out_norm=gdn_out_norm, w_out=w_out, post_mix_norm=post_mix_norm,
                   pre_mlp_norm=pre_mlp_norm, w_up=w_up, w_down=w_down, post_mlp_norm=post_mlp_norm)
    m_in = dict(pre_mix_norm=m_pre_mix_norm, w_in=m_w_in, fox_f_bias=m_fox_f_bias, fox_out_norm=m_fox_out_norm, gdn_conv_w=m_gdn_conv_w,
                gdn_a_log=m_gdn_a_log, gdn_dt_bias=m_gdn_dt_bias, gdn_out_norm=m_gdn_out_norm, w_out=m_w_out, post_mix_norm=m_post_mix_norm,
                pre_mlp_norm=m_pre_mlp_norm, w_up=m_w_up, w_down=m_w_down, post_mlp_norm=m_post_mlp_norm)
    v_in = dict(pre_mix_norm=v_pre_mix_norm, w_in=v_w_in, fox_f_bias=v_fox_f_bias, fox_out_norm=v_fox_out_norm, gdn_conv_w=v_gdn_conv_w,
                gdn_a_log=v_gdn_a_log, gdn_dt_bias=v_gdn_dt_bias, gdn_out_norm=v_gdn_out_norm, w_out=v_w_out, post_mix_norm=v_post_mix_norm,
                pre_mlp_norm=v_pre_mlp_norm, w_up=v_w_up, w_down=v_w_down, post_mlp_norm=v_post_mlp_norm)
    order_w = ("pre_mix_norm", "w_in", "fox_f_bias", "fox_out_norm", "gdn_conv_w", "gdn_a_log", "gdn_dt_bias", "gdn_out_norm", "w_out",
               "post_mix_norm", "pre_mlp_norm", "w_up", "w_down", "post_mlp_norm")
    big = ("w_in", "w_out", "w_up", "w_down")

    def row(v):
        return v if v.ndim == 2 else v.reshape(1, -1)

    cw = DPROJ // NCHIP
    win_shard = jnp.pad(w_in.astype(BF16), ((0, 0), (0, D - cw)))
    win_g, conv_g = _run_exchange(_allgather_exchange([win_shard], whole=[gdn_conv_w]), "weights_allgather_in")
    win_p = _to_padded_cols(_with_own(win_g, win_shard)[:, :, :cw].transpose(1, 0, 2).reshape(D, DPROJ))
    conv_full = _with_own(conv_g, gdn_conv_w).transpose(1, 0, 2).reshape(CONV_K, 3 * DGDN)
    late_shards = [weights[n].astype(BF16) for n in big[1:]]

    def resolve_late(gathered):
        wout_g, wup3, wdown_g = [_with_own(g, own) for g, own in zip(gathered, late_shards)]
        return wout_g.reshape(D, D), wup3, wdown_g.reshape(DFF, D)

    mlp_sums = []

    def reduce_mlp(dwup3, dwdown):
        blocks = [dwup3, dwdown.reshape(NCHIP, DFF // NCHIP, D)]
        theirs = _run_exchange(_pair_exchange(blocks), "grads_pair_exchange_mlp")
        mlp_sums.extend(_pair_sum(blocks, theirs, "grads_pair_sum_mlp"))
        return _chip_exchange(mlp_sums)

    loss_local, grad_x, (dwin_p, dwout, _, _), small, mlp_received = _local_step(
        x[0], loss_target[0], win_p, (_allgather_exchange(late_shards), resolve_late), reduce_mlp, row(pre_mix_norm),
        fox_f_bias, row(fox_out_norm), conv_full, gdn_a_log, gdn_dt_bias, row(gdn_out_norm), row(post_mix_norm),
        row(pre_mlp_norm), row(post_mlp_norm))
    loss = lax.psum(loss_local, ("x", "y", "c"))

    dwin3 = jnp.pad(_from_padded_cols(dwin_p).reshape(D, NCHIP, cw).transpose(1, 0, 2), ((0, 0), (0, 0), (0, D - cw)))
    blocks = [dwin3, dwout.reshape(NCHIP, D // NCHIP, D)]
    pair_sums = _pair_sum(blocks, _run_exchange(_pair_exchange(blocks), "grads_pair_exchange"), "grads_pair_sum")
    received = _run_exchange(_chip_exchange(pair_sums), "grads_chip_exchange")
    g_mine = _chip_sum(list(pair_sums) + mlp_sums, list(received) + list(mlp_received))
    g_theirs = _run_exchange(_pair_share(g_mine), "grads_pair_share")

    g_small, d_small, nm_small, nv_small = _small_adamw(
        _small_allreduce(*small), [row(weights[n]) for n in SMALL_NAMES], [row(m_in[n]) for n in SMALL_NAMES],
        [row(v_in[n]) for n in SMALL_NAMES])

    g_big, d_big, nm_big, nv_big = _adamw_big([weights[n] for n in big], g_mine, g_theirs, [m_in[n] for n in big],
                                              [v_in[n] for n in big])

    grads, delta, new_m, new_v = {}, {}, {}, {}
    for i, n in enumerate(big):
        grads[n], delta[n], new_m[n], new_v[n] = g_big[i], d_big[i], nm_big[i], nv_big[i]
    for i, n in enumerate(SMALL_NAMES):
        shape = weights[n].shape
        grads[n], delta[n], new_m[n], new_v[n] = (g_small[i].reshape(shape), d_small[i].reshape(shape),
                                                  nm_small[i].reshape(shape), nv_small[i].reshape(shape))
    return (loss, grad_x[None], *[grads[n] for n in order_w], *[delta[n] for n in order_w], *[new_m[n] for n in order_w],
            *[new_v[n] for n in order_w])
```

```python
import jax
import jax.numpy as jnp
from jax import lax
from jax.experimental import pallas as pl
from jax.experimental.pallas import tpu as pltpu

F32 = jnp.float32
BF16 = jnp.bfloat16
MESH = pl.DeviceIdType.MESH

S = 2048
D = 1024
NFH, FHD = 8, 64
NPAIR = NFH // 2
NGH, GHD = 4, 128
DFOX = NFH * FHD
DGDN = NGH * GHD
CHUNK = 64
NCH = S // CHUNK
CONV_K = 4
DFF = 4 * D
EPS = 1e-6
DPROJ = 3600
LANES = 128
DPROJ_PAD = 3840
BLK_GDN = 12
BLK_GZ = 24
BLK_SMALL = 28
NCHIP = 4
NDEV = 8
VMEM_LIMIT = 56 * 1024 * 1024

ADAM_LR = 0.001
ADAM_B1 = 0.9
ADAM_B2 = 0.999
ADAM_EPS = 1e-08
ADAM_WD = 0.01
ADAM_STEP = 10


def _cparams(**kw):
    return pltpu.CompilerParams(vmem_limit_bytes=VMEM_LIMIT, **kw)


def _dn(ca, cb):
    return (((ca,), (cb,)), ((), ()))


def _dot(a, b, ca=1, cb=0):
    return lax.dot_general(a.astype(BF16), b.astype(BF16), _dn(ca, cb), preferred_element_type=F32)


def _hdot(a, b, ca=1, cb=0):
    return lax.dot_general(a.astype(F32), b.astype(F32), _dn(ca, cb), precision=lax.Precision.HIGHEST,
                           preferred_element_type=F32)


@jax.custom_vjp
def _mm_nn(a, b):
    return _dot(a, b, 1, 0)


def _mm_nn_fwd(a, b):
    return _dot(a, b, 1, 0), (a, b)


def _mm_nn_bwd(res, g):
    a, b = res
    return _dot(g, b, 1, 1), _dot(a, g, 0, 0)


_mm_nn.defvjp(_mm_nn_fwd, _mm_nn_bwd)


@jax.custom_vjp
def _mm_nt(a, b):
    return _dot(a, b, 1, 1)


def _mm_nt_fwd(a, b):
    return _dot(a, b, 1, 1), (a, b)


def _mm_nt_bwd(res, g):
    a, b = res
    return _dot(g, b, 1, 0), _dot(g, a, 0, 0)


_mm_nt.defvjp(_mm_nt_fwd, _mm_nt_bwd)


@jax.custom_vjp
def _saved_inverse(m, t_inv):
    del m
    return t_inv


def _saved_inverse_fwd(m, t_inv):
    del m
    return t_inv, t_inv


def _saved_inverse_bwd(t_inv, g):
    return -_hdot(_hdot(t_inv, g, 0, 0), t_inv, 1, 1), jnp.zeros_like(t_inv)


_saved_inverse.defvjp(_saved_inverse_fwd, _saved_inverse_bwd)


def _sigmoid(z):
    return 1.0 / (1.0 + jnp.exp(-z))


def _softplus(z):
    return jnp.maximum(z, 0.0) + jnp.log(1.0 + jnp.exp(-jnp.abs(z)))


def _silu(z):
    return z * _sigmoid(z)


def _rms_scale(x):
    return lax.rsqrt(jnp.mean(x * x, axis=-1, keepdims=True) + EPS)


def _rms_bwd(x, w, g):
    r = _rms_scale(x)
    gw = g * w
    dx = r * gw - x * (r * r * r) * jnp.mean(gw * x, axis=-1, keepdims=True)
    return dx, g * x * r


def _matmul(a, b, *, name, ta=False, tb=False, tm=512, tn=512, tk=512, out_dtypes=(F32,), b3=False, o3=False,
            extra=(), epilogue=None):
    m, k = (a.shape[1], a.shape[0]) if ta else a.shape
    if b3:
        n = b.shape[1] if tb else b.shape[0] * b.shape[2]
        kb = b.shape[0] * b.shape[2] if tb else b.shape[1]
    else:
        n, kb = (b.shape[0], b.shape[1]) if tb else (b.shape[1], b.shape[0])
    assert kb == k, (name, kb, k)
    tm, tn, tk = min(tm, m), min(tn, n), min(tk, k)
    assert m % tm == 0 and n % tn == 0 and k % tk == 0, (name, m, n, k, tm, tn, tk)
    nk = k // tk
    n_extra = len(extra)
    n_out = len(out_dtypes)

    def body(*refs):
        a_ref, b_ref = refs[0], refs[1]
        extra_refs = refs[2:2 + n_extra]
        out_refs = refs[2 + n_extra:2 + n_extra + n_out]

        def finish(acc):
            outs = (acc,) if epilogue is None else epilogue(acc, *[r[...] for r in extra_refs])
            for o_ref, val in zip(out_refs, outs):
                o_ref[...] = val.astype(o_ref.dtype)

        part = _dot(a_ref[...], b_ref[...], 0 if ta else 1, 1 if tb else 0)
        if nk == 1:
            finish(part)
            return
        acc_ref = refs[-1]
        kk = pl.program_id(2)

        @pl.when(kk == 0)
        def _():
            acc_ref[...] = part

        @pl.when(kk > 0)
        def _():
            acc_ref[...] += part

        @pl.when(kk == nk - 1)
        def _():
            finish(acc_ref[...])

    a_spec = pl.BlockSpec((tk, tm), lambda i, j, kk: (kk, i)) if ta else pl.BlockSpec((tm, tk), lambda i, j, kk: (i, kk))
    if b3 and tb:
        assert b.shape[2] == tk
        b_spec = pl.BlockSpec((None, tn, tk), lambda i, j, kk: (kk, j, 0))
    elif b3:
        assert b.shape[2] == tn
        b_spec = pl.BlockSpec((None, tk, tn), lambda i, j, kk: (j, kk, 0))
    elif tb:
        b_spec = pl.BlockSpec((tn, tk), lambda i, j, kk: (j, kk))
    else:
        b_spec = pl.BlockSpec((tk, tn), lambda i, j, kk: (kk, j))
    tile = pl.BlockSpec((tm, tn), lambda i, j, kk: (i, j))
    out_specs = [tile] * n_out
    out_shape = [jax.ShapeDtypeStruct((m, n), dt) for dt in out_dtypes]
    if o3:
        out_specs[0] = pl.BlockSpec((None, tm, tn), lambda i, j, kk: (j, i, 0))
        out_shape[0] = jax.ShapeDtypeStruct((n // tn, m, tn), out_dtypes[0])
    res = pl.pallas_call(
        body, name=name, grid=(m // tm, n // tn, nk),
        in_specs=[a_spec, b_spec] + [tile] * n_extra, out_specs=out_specs, out_shape=out_shape,
        scratch_shapes=[pltpu.VMEM((tm, tn), F32)] if nk > 1 else [],
        compiler_params=_cparams(dimension_semantics=("parallel", "parallel", "arbitrary")),
    )(a, b, *extra)
    return res[0] if n_out == 1 else res


TR = 256


def _row_spec(cols):
    return pl.BlockSpec((TR, cols), lambda i: (i, 0))


def _vec_spec(cols):
    return pl.BlockSpec((1, cols), lambda i: (0, 0))


def _pre_norm(x, w):
    def body(x_ref, w_ref, h_ref):
        xv = x_ref[...]
        h_ref[...] = (xv * _rms_scale(xv) * w_ref[...]).astype(BF16)

    return pl.pallas_call(
        body, name="pre_norm", grid=(S // TR,), in_specs=[_row_spec(D), _vec_spec(D)], out_specs=_row_spec(D),
        out_shape=jax.ShapeDtypeStruct((S, D), BF16), compiler_params=_cparams(),
    )(x, w)


def _post_mix(x, mixed, w_post, w_pre_mlp):
    def body(x_ref, m_ref, wp_ref, wm_ref, x1_ref, h2_ref):
        mv = m_ref[...]
        x1 = x_ref[...] + mv * _rms_scale(mv) * wp_ref[...]
        x1_ref[...] = x1
        h2_ref[...] = (x1 * _rms_scale(x1) * wm_ref[...]).astype(BF16)

    return pl.pallas_call(
        body, name="post_mix", grid=(S // TR,),
        in_specs=[_row_spec(D), _row_spec(D), _vec_spec(D), _vec_spec(D)], out_specs=[_row_spec(D), _row_spec(D)],
        out_shape=[jax.ShapeDtypeStruct((S, D), F32), jax.ShapeDtypeStruct((S, D), BF16)], compiler_params=_cparams(),
    )(x, mixed, w_post, w_pre_mlp)


def _loss_head(x1, y, w_post_mlp, target):
    def body(x1_ref, y_ref, w_ref, t_ref, dx2_ref, dy_ref, dw_ref, loss_ref):
        i = pl.program_id(0)
        yv = y_ref[...]
        w = w_ref[...]
        x2 = x1_ref[...] + yv * _rms_scale(yv) * w
        err = x2 - t_ref[...]
        dx2 = err * (1.0 / D)
        dx2_ref[...] = dx2
        dy, dwt = _rms_bwd(yv, w, dx2)
        dy_ref[...] = dy.astype(BF16)

        @pl.when(i == 0)
        def _():
            dw_ref[...] = jnp.zeros_like(dw_ref)
            loss_ref[...] = jnp.zeros_like(loss_ref)

        dw_ref[...] += jnp.sum(dwt, axis=0, keepdims=True)
        part = 0.5 * jnp.sum(jnp.mean(err * err, axis=-1, keepdims=True), axis=0, keepdims=True)
        loss_ref[...] += jnp.broadcast_to(part, loss_ref.shape)

    return pl.pallas_call(
        body, name="loss_head", grid=(S // TR,),
        in_specs=[_row_spec(D), _row_spec(D), _vec_spec(D), _row_spec(D)],
        out_specs=[_row_spec(D), _row_spec(D), _vec_spec(D), _vec_spec(LANES)],
        out_shape=[jax.ShapeDtypeStruct((S, D), F32), jax.ShapeDtypeStruct((S, D), BF16),
                   jax.ShapeDtypeStruct((1, D), F32), jax.ShapeDtypeStruct((1, LANES), F32)],
        compiler_params=_cparams(),
    )(x1, y, w_post_mlp, target)


def _mid_bwd(dh2, x1, w_pre_mlp, dx2, mixed, w_post):
    def body(dh2_ref, x1_ref, wm_ref, dx2_ref, m_ref, wp_ref, dx1_ref, dm_ref, dwm_ref, dwp_ref):
        i = pl.program_id(0)
        dxa, dwm = _rms_bwd(x1_ref[...], wm_ref[...], dh2_ref[...])
        dx1 = dx2_ref[...] + dxa
        dx1_ref[...] = dx1
        dm, dwp = _rms_bwd(m_ref[...], wp_ref[...], dx1)
        dm_ref[...] = dm.astype(BF16)

        @pl.when(i == 0)
        def _():
            dwm_ref[...] = jnp.zeros_like(dwm_ref)
            dwp_ref[...] = jnp.zeros_like(dwp_ref)

        dwm_ref[...] += jnp.sum(dwm, axis=0, keepdims=True)
        dwp_ref[...] += jnp.sum(dwp, axis=0, keepdims=True)

    return pl.pallas_call(
        body, name="mid_bwd", grid=(S // TR,),
        in_specs=[_row_spec(D), _row_spec(D), _vec_spec(D), _row_spec(D), _row_spec(D), _vec_spec(D)],
        out_specs=[_row_spec(D), _row_spec(D), _vec_spec(D), _vec_spec(D)],
        out_shape=[jax.ShapeDtypeStruct((S, D), F32), jax.ShapeDtypeStruct((S, D), BF16),
                   jax.ShapeDtypeStruct((1, D), F32), jax.ShapeDtypeStruct((1, D), F32)],
        compiler_params=_cparams(),
    )(dh2, x1, w_pre_mlp, dx2, mixed, w_post)


def _pre_norm_bwd(dh, x, w, dx1):
    def body(dh_ref, x_ref, w_ref, dx1_ref, dx_ref, dw_ref):
        i = pl.program_id(0)
        dxa, dwt = _rms_bwd(x_ref[...], w_ref[...], dh_ref[...])
        dx_ref[...] = dx1_ref[...] + dxa

        @pl.when(i == 0)
        def _():
            dw_ref[...] = jnp.zeros_like(dw_ref)

        dw_ref[...] += jnp.sum(dwt, axis=0, keepdims=True)

    return pl.pallas_call(
        body, name="pre_norm_bwd", grid=(S // TR,),
        in_specs=[_row_spec(D), _row_spec(D), _vec_spec(D), _row_spec(D)], out_specs=[_row_spec(D), _vec_spec(D)],
        out_shape=[jax.ShapeDtypeStruct((S, D), F32), jax.ShapeDtypeStruct((1, D), F32)], compiler_params=_cparams(),
    )(dh, x, w, dx1)


BQ = 256
NQ = S // BQ
LANE_BETA, LANE_G = 8, 12


def _gate_lanes(shape):
    lane = lax.broadcasted_iota(jnp.int32, shape, 1)
    return lane < LANE_BETA, (lane >= LANE_BETA) & (lane < LANE_G), (lane >= LANE_G) & (lane < LANE_G + NGH)


def _gates(proj, bias_vec, alog_vec):
    def body(s_ref, b_ref, a_ref, o_ref, carry_ref):
        i = pl.program_id(0)

        @pl.when(i == 0)
        def _():
            carry_ref[...] = jnp.zeros_like(carry_ref)

        z = s_ref[...] + b_ref[...]
        tail = jnp.log(1.0 + jnp.exp(-jnp.abs(z)))
        sp = jnp.maximum(z, 0.0) + tail
        lf = jnp.minimum(z, 0.0) - tail
        r = lax.broadcasted_iota(jnp.int32, (BQ, BQ), 0)
        c = lax.broadcasted_iota(jnp.int32, (BQ, BQ), 1)
        tri = (c <= r).astype(F32)
        cum = _hdot(tri, lf) + carry_ref[...]
        carry_ref[...] = cum[BQ - 1:BQ, :]
        is_fox, is_beta, is_g = _gate_lanes(z.shape)
        o_ref[...] = jnp.where(is_fox, cum, jnp.where(is_beta, _sigmoid(z), jnp.where(is_g, -jnp.exp(a_ref[...]) * sp, 0.0)))

    return pl.pallas_call(
        body, name="gates", grid=(NQ,),
        in_specs=[pl.BlockSpec((BQ, LANES), lambda i: (i, BLK_SMALL)), _vec_spec(LANES), _vec_spec(LANES)],
        out_specs=pl.BlockSpec((BQ, LANES), lambda i: (i, 0)), out_shape=jax.ShapeDtypeStruct((S, LANES), F32),
        scratch_shapes=[pltpu.VMEM((1, LANES), F32)], compiler_params=_cparams(),
    )(proj, bias_vec, alog_vec)


def _gates_bwd(proj, bias_vec, alog_vec, dgates_gdn, dcum_fox, dproj):
    def body(s_ref, b_ref, a_ref, dg_ref, dc_ref, dproj_in, dproj_ref, red_ref, carry_ref):
        del dproj_in
        i = pl.program_id(0)

        @pl.when(i == 0)
        def _():
            carry_ref[...] = jnp.zeros_like(carry_ref)
            red_ref[...] = jnp.zeros_like(red_ref)

        z = s_ref[...] + b_ref[...]
        dg = dg_ref[...] + dc_ref[...]
        r = lax.broadcasted_iota(jnp.int32, (BQ, BQ), 0)
        c = lax.broadcasted_iota(jnp.int32, (BQ, BQ), 1)
        upper = (c >= r).astype(F32)
        dlf = _hdot(upper, dg) + carry_ref[...]
        carry_ref[...] = dlf[0:1, :]
        sig = _sigmoid(z)
        g_scale = -jnp.exp(a_ref[...])
        is_fox, is_beta, is_g = _gate_lanes(z.shape)
        ds = jnp.where(is_fox, dlf * (1.0 - sig), jnp.where(is_beta, dg * sig * (1.0 - sig), jnp.where(is_g, dg * g_scale * sig, 0.0)))
        dproj_ref[:, 0:LANES] = ds.astype(BF16)
        dproj_ref[:, LANES:2 * LANES] = jnp.zeros((BQ, LANES), BF16)
        dalog = jnp.where(is_g, dg * g_scale * _softplus(z), 0.0)
        sums = jnp.sum(ds, axis=0, keepdims=True)
        red_ref[0:1, :] += jnp.where(is_fox[0:1], sums, 0.0)
        red_ref[1:2, :] += pltpu.roll(jnp.where(is_g[0:1], sums, 0.0), LANES - LANE_G, 1)
        red_ref[2:3, :] += pltpu.roll(jnp.sum(dalog, axis=0, keepdims=True), LANES - LANE_G, 1)

    blk = pl.BlockSpec((BQ, LANES), lambda i: (NQ - 1 - i, 0))
    return pl.pallas_call(
        body, name="gates_bwd", grid=(NQ,),
        in_specs=[pl.BlockSpec((BQ, LANES), lambda i: (NQ - 1 - i, BLK_SMALL)), _vec_spec(LANES), _vec_spec(LANES), blk, blk,
                  pl.BlockSpec(memory_space=pl.ANY)],
        out_specs=[pl.BlockSpec((BQ, 2 * LANES), lambda i: (NQ - 1 - i, BLK_SMALL // 2)), pl.BlockSpec((8, LANES), lambda i: (0, 0))],
        out_shape=[jax.ShapeDtypeStruct((S, DPROJ_PAD), BF16), jax.ShapeDtypeStruct((8, LANES), F32)],
        input_output_aliases={5: 0},
        scratch_shapes=[pltpu.VMEM((1, LANES), F32)], compiler_params=_cparams(),
    )(proj, bias_vec, alog_vec, dgates_gdn, dcum_fox, dproj)


FOX_SCALE = FHD ** -0.5


def _head_mask(e):
    lane = lax.broadcasted_iota(jnp.int32, (1, LANES), 1)
    return (lane >= e * FHD) & (lane < (e + 1) * FHD)


def _lane_col(vals, index):
    lane = lax.broadcasted_iota(jnp.int32, vals.shape, 1)
    return jnp.sum(jnp.where(lane == index, vals, 0.0), axis=1, keepdims=True)


def _sublane_row(vals, index):
    row = lax.broadcasted_iota(jnp.int32, vals.shape, 0)
    return jnp.sum(jnp.where(row == index, vals, 0.0), axis=0, keepdims=True)


def _pair_cols(c0, c1):
    lane = lax.broadcasted_iota(jnp.int32, (c0.shape[0], 2), 1)
    return jnp.where(lane == 0, c0, c1)


def _split3(x):
    hi = x.astype(BF16).astype(F32)
    rest = x - hi
    mid = rest.astype(BF16).astype(F32)
    return hi, mid, (rest - mid).astype(BF16).astype(F32)


def _fox_operand(vals, e, cum, is_query):
    lane = lax.broadcasted_iota(jnp.int32, (1, LANES), 1)
    base = (1 - e) * FHD
    parts = _split3(cum)
    own = jnp.where(_head_mask(e), vals * FOX_SCALE if is_query else vals, 0.0)
    cum_at, ones_at = (base, base + 3) if is_query else (base + 3, base)
    sign = 1.0 if is_query else -1.0
    out = own + jnp.where((lane >= ones_at) & (lane < ones_at + 3), 1.0, 0.0)
    for i, part in enumerate(parts):
        out = out + jnp.where(lane == cum_at + i, sign * part, 0.0)
    return out.astype(BF16)


def _causal_block():
    return lax.broadcasted_iota(jnp.int32, (BQ, BQ), 1) <= lax.broadcasted_iota(jnp.int32, (BQ, BQ), 0)


def _head_rms(o, masks):
    o2 = o * o
    r = [lax.rsqrt(jnp.sum(jnp.where(mk, o2, 0.0), axis=1, keepdims=True) * (1.0 / FHD) + EPS) for mk in masks]
    return jnp.where(masks[0], r[0], r[1])


def _hosted(exchange):
    if exchange is None:
        return [], [], [], [], []
    return (exchange.inputs, [HBM] * len(exchange.inputs), [HBM] * len(exchange.out_shape), exchange.out_shape,
            exchange.sem_shapes())


def _fox_fwd(proj, gates, w2, exchange=None):
    ex_in, ex_in_specs, ex_out_specs, ex_out_shape, ex_scratch = _hosted(exchange)

    def body(*refs):
        q_ref, k_ref, v_ref, g_ref, w_ref = refs[:5]
        mix_ref, o_ref, lse_ref = refs[5 + len(ex_in):8 + len(ex_in)]
        ka_ref, vb_ref = refs[8 + len(ex_in) + len(ex_out_shape):10 + len(ex_in) + len(ex_out_shape)]
        ex_refs = refs[5:5 + len(ex_in)] + refs[8 + len(ex_in):8 + len(ex_in) + len(ex_out_shape)] + refs[-2:]
        hp, qi = pl.program_id(0), pl.program_id(1)

        if exchange is not None:
            @pl.when((hp == 0) & (qi == 0))
            def _():
                exchange.start(*exchange.split(ex_refs))

        @pl.when(qi == 0)
        def _():
            kv = k_ref[...]
            gt = g_ref[...]
            for e in range(2):
                ka_ref[e] = _fox_operand(kv, e, _lane_col(gt, 2 * hp + e), False)
            vb_ref[...] = v_ref[...].astype(BF16)

        masks = [_head_mask(0), _head_mask(1)]
        qblk = q_ref[...]
        gt = g_ref[pl.ds(pl.multiple_of(qi * BQ, BQ), BQ), :]
        qs = [_fox_operand(qblk, e, _lane_col(gt, 2 * hp + e), True) for e in range(2)]

        def block(kj, carry, diagonal):
            rows = pl.ds(pl.multiple_of(kj * BQ, BQ), BQ)
            vb = vb_ref[rows, :]
            new = []
            for e in range(2):
                m, l, acc = carry[e]
                s = _dot(qs[e], ka_ref[e, rows, :], 1, 1)
                if diagonal:
                    s = jnp.where(_causal_block(), s, -jnp.inf)
                m_new = jnp.maximum(m, jnp.max(s, axis=-1, keepdims=True))
                p = jnp.exp(s - m_new)
                alpha = jnp.exp(m - m_new)
                new.append((m_new, alpha * l + jnp.sum(p, axis=-1, keepdims=True), alpha * acc + _dot(p, vb)))
            return tuple(new)

        one = (jnp.full((BQ, 1), -jnp.inf, F32), jnp.zeros((BQ, 1), F32), jnp.zeros((BQ, LANES), F32))
        below = lax.fori_loop(0, qi, lambda kj, carry: block(kj, carry, False), (one, one))
        (m0, l0, a0), (m1, l1, a1) = block(qi, below, True)
        o = jnp.where(masks[0], a0 / l0, a1 / l1)
        o_ref[...] = o
        mix_ref[...] = (o * _head_rms(o, masks) * w_ref[...]).astype(BF16)
        lse_ref[...] = _pair_cols(m0 + jnp.log(l0), m1 + jnp.log(l1))

        if exchange is not None:
            @pl.when((hp == NPAIR - 1) & (qi == NQ - 1))
            def _():
                exchange.finish(*exchange.split(ex_refs))

    def col(part):
        return pl.BlockSpec((S, LANES), lambda hp, i: (0, 3 * hp + part))

    blk = pl.BlockSpec((BQ, LANES), lambda hp, i: (i, hp))
    res = pl.pallas_call(
        body, name="fox_fwd", grid=(NPAIR, NQ),
        in_specs=[pl.BlockSpec((BQ, LANES), lambda hp, i: (i, 3 * hp)), col(1), col(2),
                  pl.BlockSpec((S, LANES), lambda hp, i: (0, 0)), pl.BlockSpec((1, LANES), lambda hp, i: (0, 0))] + ex_in_specs,
        out_specs=[blk, blk, pl.BlockSpec((None, BQ, 2), lambda hp, i: (hp, i, 0))] + ex_out_specs,
        out_shape=[jax.ShapeDtypeStruct((S, D), BF16), jax.ShapeDtypeStruct((S, DFOX), F32),
                   jax.ShapeDtypeStruct((NPAIR, S, 2), F32)] + ex_out_shape,
        scratch_shapes=[pltpu.VMEM((2, S, LANES), BF16), pltpu.VMEM((S, LANES), BF16)] + ex_scratch,
        compiler_params=_cparams(),
    )(proj, proj, proj, gates, w2, *ex_in)
    return res[0], res[1], res[2], res[3:]


def _fox_norm_bwd(o, dmix, w2):
    def body(o_ref, g_ref, w_ref, do_ref, dl_ref, dw_ref):
        hp, qi = pl.program_id(0), pl.program_id(1)
        masks = [_head_mask(0), _head_mask(1)]
        ov = o_ref[...]
        g = g_ref[...]
        r = _head_rms(ov, masks)
        gw = g * w_ref[...]
        gwo = gw * ov
        mean = [jnp.sum(jnp.where(mk, gwo, 0.0), axis=1, keepdims=True) * (1.0 / FHD) for mk in masks]
        do = r * gw - ov * (r * r * r) * jnp.where(masks[0], mean[0], mean[1])
        do_ref[...] = do.astype(BF16)
        doo = do * ov
        dl_ref[...] = _pair_cols(*[jnp.sum(jnp.where(mk, doo, 0.0), axis=1, keepdims=True) for mk in masks])

        @pl.when((hp == 0) & (qi == 0))
        def _():
            dw_ref[...] = jnp.zeros_like(dw_ref)

        dw_ref[...] += jnp.sum(g * ov * r, axis=0, keepdims=True)

        @pl.when((hp == NPAIR - 1) & (qi == NQ - 1))
        def _():
            dw = dw_ref[...]
            dw_ref[...] = dw + pltpu.roll(dw, FHD, 1)

    blk = pl.BlockSpec((BQ, LANES), lambda hp, i: (i, hp))
    vec = pl.BlockSpec((1, LANES), lambda hp, i: (0, 0))
    return pl.pallas_call(
        body, name="fox_norm_bwd", grid=(NPAIR, NQ), in_specs=[blk, blk, vec],
        out_specs=[blk, pl.BlockSpec((None, BQ, 2), lambda hp, i: (hp, i, 0)), vec],
        out_shape=[jax.ShapeDtypeStruct((S, DFOX), BF16), jax.ShapeDtypeStruct((NPAIR, S, 2), F32),
                   jax.ShapeDtypeStruct((1, LANES), F32)],
        compiler_params=_cparams(),
    )(o, dmix, w2)


def _fox_bwd(proj, do, gates, lse, delta, exchange=None):
    ex_in, ex_in_specs, ex_out_specs, ex_out_shape, ex_scratch = _hosted(exchange)

    def body(*refs):
        q_ref, k_ref, v_ref, do_ref, g_ref, lse_ref, dl_ref = refs[:7]
        dproj_ref, dc_ref = refs[7 + len(ex_in):9 + len(ex_in)]
        qa_ref, dq_ref = refs[9 + len(ex_in) + len(ex_out_shape):11 + len(ex_in) + len(ex_out_shape)]
        ex_refs = refs[7:7 + len(ex_in)] + refs[9 + len(ex_in):9 + len(ex_in) + len(ex_out_shape)] + refs[-2:]
        hp, kj = pl.program_id(0), pl.program_id(1)

        if exchange is not None:
            @pl.when((hp == 0) & (kj == 0))
            def _():
                exchange.start(*exchange.split(ex_refs))

        @pl.when(kj == 0)
        def _():
            qv = q_ref[...]
            gt = g_ref[...]
            for e in range(2):
                qa_ref[e] = _fox_operand(qv, e, _lane_col(gt, 2 * hp + e), True)
            dq_ref[...] = jnp.zeros_like(dq_ref)

        @pl.when((hp == 0) & (kj == 0))
        def _():
            dc_ref[...] = jnp.zeros_like(dc_ref)

        masks = [_head_mask(0), _head_mask(1)]
        krows = pl.ds(pl.multiple_of(kj * BQ, BQ), BQ)
        kv = k_ref[...]
        gk = g_ref[krows, :]
        kas = [_fox_operand(kv, e, _lane_col(gk, 2 * hp + e), False) for e in range(2)]
        vb = v_ref[...].astype(BF16)
        lane = lax.broadcasted_iota(jnp.int32, (BQ, LANES), 1)

        def block(qi, carry, diagonal):
            dk, dv, cs = carry
            rows = pl.ds(pl.multiple_of(qi * BQ, BQ), BQ)
            dov = do_ref[rows, :]
            lse2 = lse_ref[rows, :]
            dl2 = dl_ref[rows, :]
            dq = jnp.zeros((BQ, LANES), F32)
            dc = jnp.zeros((BQ, LANES), F32)
            cs_new = []
            for e in range(2):
                qa = qa_ref[e, rows, :]
                s = _dot(qa, kas[e], 1, 1)
                if diagonal:
                    s = jnp.where(_causal_block(), s, -jnp.inf)
                doe = jnp.where(masks[e], dov, jnp.zeros_like(dov))
                p = jnp.exp(s - _lane_col(lse2, e))
                dp = _dot(doe, vb, 1, 1)
                ds = p * (dp - _lane_col(dl2, e))
                dv = dv + _dot(p, doe, 0, 0)
                dk = dk + _dot(ds, jnp.where(masks[e], qa, jnp.zeros_like(qa)), 0, 0)
                dq = dq + jnp.where(masks[e], _dot(ds, kas[e]), 0.0) * FOX_SCALE
                cs_new.append(cs[e] + jnp.sum(ds, axis=0, keepdims=True))
                dc = dc + jnp.where(lane == 2 * hp + e, jnp.sum(ds, axis=1, keepdims=True), 0.0)
            dq_ref[rows, :] += dq
            dc_ref[rows, :] += dc
            return dk, dv, tuple(cs_new)

        zero = jnp.zeros((BQ, LANES), F32)
        first = block(kj, (zero, zero, (jnp.zeros((1, BQ), F32), jnp.zeros((1, BQ), F32))), True)
        dk, dv, cs = lax.fori_loop(kj + 1, NQ, lambda qi, carry: block(qi, carry, False), first)
        dproj_ref[krows, LANES:2 * LANES] = dk.astype(BF16)
        dproj_ref[krows, 2 * LANES:3 * LANES] = dv.astype(BF16)
        r = lax.broadcasted_iota(jnp.int32, (BQ, BQ), 0)
        c = lax.broadcasted_iota(jnp.int32, (BQ, BQ), 1)
        dcol = jnp.zeros((BQ, LANES), F32)
        for e in range(2):
            col = jnp.sum(jnp.where(r == c, cs[e], 0.0), axis=1, keepdims=True)
            dcol = dcol + jnp.where(lane == 2 * hp + e, col, 0.0)
        dc_ref[krows, :] -= dcol

        @pl.when(kj == NQ - 1)
        def _():
            dproj_ref[:, 0:LANES] = dq_ref[...].astype(BF16)

        if exchange is not None:
            @pl.when((hp == NPAIR - 1) & (kj == NQ - 1))
            def _():
                exchange.finish(*exchange.split(ex_refs))

    def blk(part):
        return pl.BlockSpec((BQ, LANES), lambda hp, j: (j, 3 * hp + part))

    pair = pl.BlockSpec((None, S, 2), lambda hp, j: (hp, 0, 0))
    res = pl.pallas_call(
        body, name="fox_bwd", grid=(NPAIR, NQ),
        in_specs=[pl.BlockSpec((S, LANES), lambda hp, j: (0, 3 * hp)), blk(1), blk(2),
                  pl.BlockSpec((S, LANES), lambda hp, j: (0, hp)), pl.BlockSpec((S, LANES), lambda hp, j: (0, 0)),
                  pair, pair] + ex_in_specs,
        out_specs=[pl.BlockSpec((S, 3 * LANES), lambda hp, j: (0, hp)), pl.BlockSpec((S, LANES), lambda hp, j: (0, 0))]
        + ex_out_specs,
        out_shape=[jax.ShapeDtypeStruct((S, DPROJ_PAD), BF16), jax.ShapeDtypeStruct((S, LANES), F32)] + ex_out_shape,
        scratch_shapes=[pltpu.VMEM((2, S, LANES), BF16), pltpu.VMEM((S, LANES), F32)] + ex_scratch,
        compiler_params=_cparams(),
    )(proj, proj, proj, do, gates, lse, delta, *ex_in)
    return res[0], res[1], res[2:]


NQKV = 3 * NGH
GDN_QSCALE = GHD ** -0.5


def _shift_down(x, s):
    if s == 0:
        return x
    row = lax.broadcasted_iota(jnp.int32, x.shape, 0)
    return jnp.where(row >= s, pltpu.roll(x, s, 0), 0.0)


def _shift_up(x, s):
    if s == 0:
        return x
    n = x.shape[0]
    row = lax.broadcasted_iota(jnp.int32, x.shape, 0)
    return jnp.where(row < n - s, pltpu.roll(x, n - s, 0), 0.0)


def _conv_pre(xv, wv):
    pre = xv * wv[CONV_K - 1:CONV_K, :]
    for j in range(CONV_K - 1):
        pre = pre + _shift_down(xv, CONV_K - 1 - j) * wv[j:j + 1, :]
    return pre


def _l2_factors(b):
    return b < 2 * NGH, jnp.where(b < NGH, GDN_QSCALE, 1.0)


def _gdn_pre(proj, conv_w):
    def body(x_ref, w_ref, o_ref):
        b = pl.program_id(0)
        c = _silu(_conv_pre(x_ref[...], w_ref[...]))
        normed, scale = _l2_factors(b)
        rs = lax.rsqrt(jnp.sum(c * c, axis=-1, keepdims=True) + EPS)
        o_ref[...] = c * jnp.where(normed, rs, 1.0) * scale

    return pl.pallas_call(
        body, name="gdn_pre", grid=(NQKV,),
        in_specs=[pl.BlockSpec((S, GHD), lambda b: (0, BLK_GDN + b)), pl.BlockSpec((CONV_K, GHD), lambda b: (0, b))],
        out_specs=pl.BlockSpec((S, GHD), lambda b: (0, b)),
        out_shape=jax.ShapeDtypeStruct((S, NQKV * GHD), F32), compiler_params=_cparams(),
    )(proj, conv_w)


def _gdn_pre_bwd(proj, conv_w, dqkv, dproj):
    def body(x_ref, w_ref, dy_ref, dproj_in, dx_ref, dw_ref):
        del dproj_in
        b = pl.program_id(0)
        xv = x_ref[...]
        wv = w_ref[...]
        pre = _conv_pre(xv, wv)
        sig = _sigmoid(pre)
        c = pre * sig
        normed, scale = _l2_factors(b)
        g = dy_ref[...] * scale
        rs = lax.rsqrt(jnp.sum(c * c, axis=-1, keepdims=True) + EPS)
        dc_n = rs * g - c * (rs * rs * rs) * jnp.sum(g * c, axis=-1, keepdims=True)
        dc = jnp.where(normed, dc_n, g)
        dpre = dc * sig * (1.0 + pre * (1.0 - sig))
        dx = dpre * wv[CONV_K - 1:CONV_K, :]
        for j in range(CONV_K - 1):
            dx = dx + _shift_up(dpre, CONV_K - 1 - j) * wv[j:j + 1, :]
        dx_ref[...] = dx.astype(BF16)
        for j in range(CONV_K):
            dw_ref[j:j + 1, :] = jnp.sum(dpre * _shift_down(xv, CONV_K - 1 - j), axis=0, keepdims=True)

    return pl.pallas_call(
        body, name="gdn_pre_bwd", grid=(NQKV,),
        in_specs=[pl.BlockSpec((S, GHD), lambda b: (0, BLK_GDN + b)), pl.BlockSpec((CONV_K, GHD), lambda b: (0, b)),
                  pl.BlockSpec((None, S, GHD), lambda b: (b // NGH, 0, b % NGH)), pl.BlockSpec(memory_space=pl.ANY)],
        out_specs=[pl.BlockSpec((S, GHD), lambda b: (0, BLK_GDN + b)), pl.BlockSpec((CONV_K, GHD), lambda b: (0, b))],
        out_shape=[jax.ShapeDtypeStruct((S, DPROJ_PAD), BF16), jax.ShapeDtypeStruct((CONV_K, NQKV * GHD), F32)],
        input_output_aliases={3: 0}, compiler_params=_cparams(),
    )(proj, conv_w, dqkv, dproj)


CB = 4
NCB = NCH // CB


def _chunk_prep(qs, ks, vs, gcols, bcols, t_saved=None):
    n = range(len(qs))
    r = lax.broadcasted_iota(jnp.int32, (CHUNK, CHUNK), 0)
    c = lax.broadcasted_iota(jnp.int32, (CHUNK, CHUNK), 1)
    incl = c <= r
    eye = (r == c).astype(F32)
    grow = [jnp.sum(gcols[i] * eye, axis=0, keepdims=True) for i in n]
    gc_col = [jnp.sum(jnp.where(incl, grow[i], 0.0), axis=1, keepdims=True) for i in n]
    gc_row = [jnp.sum(jnp.where(r <= c, gcols[i], 0.0), axis=0, keepdims=True) for i in n]
    decay = [jnp.exp(jnp.where(incl, gc_col[i] - gc_row[i], -jnp.inf)) for i in n]
    kb = [ks[i] * bcols[i] for i in n]
    vb = [vs[i] * bcols[i] for i in n]
    kk = [_mm_nt(kb[i], ks[i]) for i in n]
    m = [jnp.where(c < r, kk[i] * decay[i], 0.0) for i in n]
    if t_saved is None:
        t_inv = [eye - m[i] for i in n]
        p = [_hdot(m[i], m[i]) for i in n]
        for step in range(5):
            t_inv = [t_inv[i] + _hdot(t_inv[i], p[i]) for i in n]
            if step < 4:
                p = [_hdot(p[i], p[i]) for i in n]
    else:
        t_inv = [_saved_inverse(m[i], t_saved[i]) for i in n]
    egc = [jnp.exp(gc_col[i]) for i in n]
    u = [_mm_nn(t_inv[i], vb[i]) for i in n]
    w = [_mm_nn(t_inv[i], kb[i] * egc[i]) for i in n]
    qk = [_mm_nt(qs[i], ks[i]) for i in n]
    gc_last = [gc_col[i][CHUNK - 1:CHUNK, :] for i in n]
    return [(u[i], w[i], qk[i] * decay[i], qs[i] * egc[i], ks[i] * jnp.exp(gc_last[i] - gc_col[i]), jnp.exp(gc_last[i]),
             t_inv[i]) for i in n]


def _prep_specs():
    rows = CB * CHUNK
    qs = pl.BlockSpec((rows, GHD), lambda i, h: (i, h))
    ks = pl.BlockSpec((rows, GHD), lambda i, h: (i, NGH + h))
    vs = pl.BlockSpec((rows, GHD), lambda i, h: (i, 2 * NGH + h))
    gs = pl.BlockSpec((rows, LANES), lambda i, h: (i, 0))
    a_s = pl.BlockSpec((None, rows, CHUNK), lambda i, h: (h, i, 0))
    gl_s = pl.BlockSpec((None, CB, 1, LANES), lambda i, h: (h, i, 0, 0))
    return qs, ks, vs, gs, a_s, gl_s


def _gdn_prep(qkv, gates):
    def body(q_ref, k_ref, v_ref, g_ref, u_ref, w_ref, qd_ref, kd_ref, a_ref, gl_ref, t_ref):
        h = pl.program_id(1)
        chunks = [pl.ds(cidx * CHUNK, CHUNK) for cidx in range(CB)]
        gts = [g_ref[rows, :] for rows in chunks]
        outs = _chunk_prep([q_ref[rows, :] for rows in chunks], [k_ref[rows, :] for rows in chunks],
                           [v_ref[rows, :] for rows in chunks], [_lane_col(gt, LANE_G + h) for gt in gts],
                           [_lane_col(gt, LANE_BETA + h) for gt in gts])
        for cidx, rows in enumerate(chunks):
            u, w, a, qd, kd, gl, t_inv = outs[cidx]
            u_ref[rows, :] = u
            w_ref[rows, :] = w
            qd_ref[rows, :] = qd
            kd_ref[rows, :] = kd
            a_ref[rows, :] = a
            t_ref[rows, :] = t_inv
            gl_ref[cidx] = jnp.broadcast_to(gl, (1, LANES))

    qs, ks, vs, gs, a_s, gl_s = _prep_specs()
    tok = jax.ShapeDtypeStruct((S, DGDN), F32)
    sq = jax.ShapeDtypeStruct((NGH, S, CHUNK), F32)
    return pl.pallas_call(
        body, name="gdn_prep", grid=(NCB, NGH), in_specs=[qs, ks, vs, gs], out_specs=[qs, qs, qs, qs, a_s, gl_s, a_s],
        out_shape=[tok, tok, tok, tok, sq, jax.ShapeDtypeStruct((NGH, NCH, 1, LANES), F32), sq],
        compiler_params=_cparams(),
    )(qkv, qkv, qkv, gates)


def _gdn_prep_bwd(qkv, gates, t_inv, du, dw, dqd, dkd, da, dgl):
    def body(q_ref, k_ref, v_ref, g_ref, t_ref, du_ref, dw_ref, dqd_ref, dkd_ref, da_ref, dgl_ref, dqkv_ref, dg_ref):
        h = pl.program_id(1)

        @pl.when(h == 0)
        def _():
            dg_ref[...] = jnp.zeros_like(dg_ref)

        lane = lax.broadcasted_iota(jnp.int32, (CHUNK, LANES), 1)
        chunks = [pl.ds(cidx * CHUNK, CHUNK) for cidx in range(CB)]
        gts = [g_ref[rows, :] for rows in chunks]
        t_saved = [t_ref[rows, :] for rows in chunks]
        _, vjp = jax.vjp(lambda *args: [o[:6] for o in _chunk_prep(*args, t_saved=t_saved)],
                         [q_ref[rows, :] for rows in chunks], [k_ref[rows, :] for rows in chunks],
                         [v_ref[rows, :] for rows in chunks], [_lane_col(gt, LANE_G + h) for gt in gts],
                         [_lane_col(gt, LANE_BETA + h) for gt in gts])
        dqs, dks, dvs, dgcs, dbcs = vjp([(du_ref[rows, :], dw_ref[rows, :], da_ref[rows, :], dqd_ref[rows, :],
                                          dkd_ref[rows, :], dgl_ref[cidx][:, 0:1]) for cidx, rows in enumerate(chunks)])
        for cidx, rows in enumerate(chunks):
            dq, dk, dv, dgc, dbc = dqs[cidx], dks[cidx], dvs[cidx], dgcs[cidx], dbcs[cidx]
            dqkv_ref[0, rows, :] = dq
            dqkv_ref[1, rows, :] = dk
            dqkv_ref[2, rows, :] = dv
            dg_ref[rows, :] += jnp.where(lane == LANE_G + h, dgc, 0.0) + jnp.where(lane == LANE_BETA + h, dbc, 0.0)

    qs, ks, vs, gs, a_s, gl_s = _prep_specs()
    return pl.pallas_call(
        body, name="gdn_prep_bwd", grid=(NCB, NGH), in_specs=[qs, ks, vs, gs, a_s, qs, qs, qs, qs, a_s, gl_s],
        out_specs=[pl.BlockSpec((3, CB * CHUNK, GHD), lambda i, h: (0, i, h)), gs],
        out_shape=[jax.ShapeDtypeStruct((3, S, DGDN), F32), jax.ShapeDtypeStruct((S, LANES), F32)],
        compiler_params=_cparams(),
    )(qkv, qkv, qkv, gates, t_inv, du, dw, dqd, dkd, da, dgl)


def _scan_specs():
    hs = pl.BlockSpec((S, GHD), lambda h: (0, h))
    a_s = pl.BlockSpec((None, S, CHUNK), lambda h: (h, 0, 0))
    gl_s = pl.BlockSpec((None, NCH, 1, LANES), lambda h: (h, 0, 0, 0))
    st_s = pl.BlockSpec((None, NCH, GHD, GHD), lambda h: (h, 0, 0, 0))
    gz_s = pl.BlockSpec((S, GHD), lambda h: (0, BLK_GZ + h))
    mix_s = pl.BlockSpec((S, GHD), lambda h: (0, NPAIR + h))
    return hs, a_s, gl_s, st_s, gz_s, mix_s


def _gdn_scan(u, w, qd, kd, a, gl, proj, w_norm, mix):
    def body(u_ref, w_ref, qd_ref, kd_ref, a_ref, gl_ref, z_ref, wn_ref, mix_in, mix_ref, o_ref, st_ref):
        del mix_in

        def step(ci, state):
            rows = pl.ds(pl.multiple_of(ci * CHUNK, CHUNK), CHUNK)
            st_ref[ci] = state
            vn = u_ref[rows, :] - _dot(w_ref[rows, :], state)
            o_ref[rows, :] = _dot(qd_ref[rows, :], state) + _dot(a_ref[rows, :], vn)
            return state * gl_ref[ci] + _dot(kd_ref[rows, :], vn, 0, 0)

        lax.fori_loop(0, NCH, step, jnp.zeros((GHD, GHD), F32))
        ov = o_ref[...]
        mix_ref[...] = (ov * _rms_scale(ov) * wn_ref[...] * _silu(z_ref[...])).astype(BF16)

    hs, a_s, gl_s, st_s, gz_s, mix_s = _scan_specs()
    return pl.pallas_call(
        body, name="gdn_scan", grid=(NGH,),
        in_specs=[hs, hs, hs, hs, a_s, gl_s, gz_s, pl.BlockSpec((1, GHD), lambda h: (0, 0)), pl.BlockSpec(memory_space=pl.ANY)],
        out_specs=[mix_s, hs, st_s],
        out_shape=[jax.ShapeDtypeStruct((S, D), BF16), jax.ShapeDtypeStruct((S, DGDN), F32),
                   jax.ShapeDtypeStruct((NGH, NCH, GHD, GHD), F32)],
        input_output_aliases={8: 0}, compiler_params=_cparams(),
    )(u, w, qd, kd, a, gl, proj, w_norm, mix)


def _gdn_scan_bwd(dmix, o, proj, w_norm, u, w, qd, kd, a, gl, states, dproj):
    def body(dy_ref, o_ref, z_ref, wn_ref, u_ref, w_ref, qd_ref, kd_ref, a_ref, gl_ref, st_ref, dproj_in,
             dz_ref, du_ref, dw_ref, dqd_ref, dkd_ref, da_ref, dgl_ref, dwn_ref, do_ref):
        del dproj_in
        h = pl.program_id(0)
        ov = o_ref[...]
        zv = z_ref[...]
        wn = wn_ref[...]
        g = dy_ref[...]
        sig = _sigmoid(zv)
        dz_ref[...] = (g * (ov * _rms_scale(ov) * wn) * sig * (1.0 + zv * (1.0 - sig))).astype(BF16)
        do, dwt = _rms_bwd(ov, wn, g * zv * sig)
        do_ref[...] = do

        @pl.when(h == 0)
        def _():
            dwn_ref[...] = jnp.zeros_like(dwn_ref)

        dwn_ref[...] += jnp.sum(dwt, axis=0, keepdims=True)

        def step(t, dstate):
            ci = NCH - 1 - t
            rows = pl.ds(pl.multiple_of(ci * CHUNK, CHUNK), CHUNK)
            state = st_ref[ci]
            dov = do_ref[rows, :]
            wv = w_ref[rows, :]
            kdv = kd_ref[rows, :]
            vn = u_ref[rows, :] - _dot(wv, state)
            dvn = _dot(a_ref[rows, :], dov, 0, 0) + _dot(kdv, dstate)
            da_ref[rows, :] = _dot(dov, vn, 1, 1)
            dqd_ref[rows, :] = _dot(dov, state, 1, 1)
            dkd_ref[rows, :] = _dot(vn, dstate, 1, 1)
            dgl = jnp.sum(jnp.sum(dstate * state, axis=1, keepdims=True), axis=0, keepdims=True)
            dgl_ref[ci] = jnp.broadcast_to(dgl, (1, LANES))
            du_ref[rows, :] = dvn
            dw_ref[rows, :] = -_dot(dvn, state, 1, 1)
            return dstate * gl_ref[ci] + _dot(qd_ref[rows, :], dov, 0, 0) - _dot(wv, dvn, 0, 0)

        lax.fori_loop(0, NCH, step, jnp.zeros((GHD, GHD), F32))

    hs, a_s, gl_s, st_s, gz_s, mix_s = _scan_specs()
    vec = pl.BlockSpec((1, GHD), lambda h: (0, 0))
    tok = jax.ShapeDtypeStruct((S, DGDN), F32)
    return pl.pallas_call(
        body, name="gdn_scan_bwd", grid=(NGH,),
        in_specs=[mix_s, hs, gz_s, vec, hs, hs, hs, hs, a_s, gl_s, st_s, pl.BlockSpec(memory_space=pl.ANY)],
        out_specs=[gz_s, hs, hs, hs, hs, a_s, gl_s, vec],
        out_shape=[jax.ShapeDtypeStruct((S, DPROJ_PAD), BF16), tok, tok, tok, tok,
                   jax.ShapeDtypeStruct((NGH, S, CHUNK), F32), jax.ShapeDtypeStruct((NGH, NCH, 1, LANES), F32),
                   jax.ShapeDtypeStruct((1, GHD), F32)],
        input_output_aliases={11: 0}, scratch_shapes=[pltpu.VMEM((S, GHD), F32)], compiler_params=_cparams(),
    )(dmix, o, proj, w_norm, u, w, qd, kd, a, gl, states, dproj)


def _place():
    return lax.axis_index("x"), lax.axis_index("y"), lax.axis_index("c")


def _other_chips(x, y):
    return [(1 - x, y), (x, 1 - y), (1 - x, 1 - y)]


HBM = pl.BlockSpec(memory_space=pltpu.HBM)
VMEM = pl.BlockSpec(memory_space=pltpu.VMEM)


def _half_rows(ref_or_rows, half):
    rows = ref_or_rows // 2
    return pl.ds(pl.multiple_of(half * rows, rows), rows)


class _Exchange:
    def __init__(self, inputs, out_shape, n_sems, start, finish):
        self.inputs, self.out_shape, self.n_sems, self.start, self.finish = inputs, out_shape, n_sems, start, finish

    def sem_shapes(self):
        return [pltpu.SemaphoreType.DMA((self.n_sems,)), pltpu.SemaphoreType.DMA((self.n_sems,))]

    def split(self, refs):
        n_in, n_out = len(self.inputs), len(self.out_shape)
        return refs[:n_in], refs[n_in:n_in + n_out], refs[n_in + n_out], refs[n_in + n_out + 1]


def _run_exchange(ex, name):
    def body(*refs):
        parts = ex.split(refs)
        ex.start(*parts)
        ex.finish(*parts)

    return pl.pallas_call(
        body, name=name, in_specs=[HBM] * len(ex.inputs), out_specs=[HBM] * len(ex.out_shape), out_shape=ex.out_shape,
        scratch_shapes=ex.sem_shapes(), compiler_params=_cparams(),
    )(*ex.inputs)


def _allgather_exchange(shards, whole=()):
    n, nw = len(shards), len(whole)

    def plan(src, outs, send_sems, recv_sems):
        x, y, c = _place()
        chips = _other_chips(x, y)
        chip_ids = [2 * ch[0] + ch[1] for ch in chips]

        def copy(a, k, chip_index, half, to, from_src):
            rows = _half_rows(src[a].shape[0], half)
            dst = outs[a].at[chip_index, rows]
            return pltpu.make_async_remote_copy(
                src_ref=src[a].at[rows] if from_src else dst, dst_ref=dst, send_sem=send_sems.at[6 * a + k],
                recv_sem=recv_sems.at[6 * a + k], device_id=to, device_id_type=MESH)

        def whole_copy(b, k, chip_index, to):
            return pltpu.make_async_remote_copy(
                src_ref=src[n + b], dst_ref=outs[n + b].at[chip_index], send_sem=send_sems.at[6 * n + 3 * b + k],
                recv_sem=recv_sems.at[6 * n + 3 * b + k], device_id=to, device_id_type=MESH)

        me, sibling = (x, y, c), (x, y, 1 - c)
        first = [copy(a, j, 2 * x + y, c, (*chips[j], c), True) for a in range(n) for j in range(3)]
        first += [whole_copy(b, j, 2 * x + y, (*chips[j], c)) for b in range(nw) for j in range(3)]
        landing = [copy(a, j, chip_ids[j], c, me, False) for a in range(n) for j in range(3)]
        passed = [copy(a, 3 + j, chip_ids[j], c, sibling, False) for a in range(n) for j in range(3)]
        arriving = [copy(a, 3 + j, chip_ids[j], 1 - c, me, False) for a in range(n) for j in range(3)]
        arriving += [whole_copy(b, j, chip_ids[j], me) for b in range(nw) for j in range(3)]
        return first, landing, passed, arriving

    def start(*refs):
        for cp in plan(*refs)[0]:
            cp.start()

    def finish(*refs):
        first, landing, passed, arriving = plan(*refs)
        for lands, onward in zip(landing, passed):
            lands.wait_recv()
            onward.start()
        for cp in arriving:
            cp.wait_recv()
        for cp in first + passed:
            cp.wait_send()

    out_shape = [jax.ShapeDtypeStruct((NCHIP,) + s.shape, s.dtype) for s in list(shards) + list(whole)]
    return _Exchange(list(shards) + list(whole), out_shape, 6 * n + 3 * nw, start, finish)


def _with_own(gathered, own):
    x, y, _ = _place()
    return lax.dynamic_update_index_in_dim(gathered, own, 2 * x + y, axis=0)


def _simple_exchange(inputs, out_shape, copies_of):
    def start(*refs):
        for cp in copies_of(*refs):
            cp.start()

    def finish(*refs):
        for cp in copies_of(*refs):
            cp.wait()

    return _Exchange(list(inputs), out_shape, len(out_shape) * 3, start, finish)


def _pair_exchange(grads):
    def copies_of(src, outs, send_sems, recv_sems):
        x, y, c = _place()
        return [pltpu.make_async_remote_copy(
            src_ref=src[a].at[:, _half_rows(src[a].shape[1], 1 - c)], dst_ref=outs[a], send_sem=send_sems.at[a],
            recv_sem=recv_sems.at[a], device_id=(x, y, 1 - c), device_id_type=MESH) for a in range(len(src))]

    return _simple_exchange(
        grads, [jax.ShapeDtypeStruct((g.shape[0], g.shape[1] // 2, g.shape[2]), g.dtype) for g in grads], copies_of)


def _pair_sum(grads, theirs, name):
    n = len(grads)

    def body(*refs):
        south = lax.axis_index("c") == 0
        for a in range(n):
            g = refs[a][...]
            half = g.shape[0] // 2
            mine = jnp.where(south, g[:half], g[half:])
            refs[2 * n + a][...] = (mine.astype(F32) + refs[n + a][...].astype(F32)).astype(BF16)

    def specs(arrs):
        return [pl.BlockSpec((None,) + g.shape[1:], lambda j: (j, 0, 0)) for g in arrs]

    return pl.pallas_call(
        body, name=name, grid=(NCHIP,), in_specs=specs(grads) + specs(theirs), out_specs=specs(theirs),
        out_shape=[jax.ShapeDtypeStruct(g.shape, BF16) for g in theirs], compiler_params=_cparams(),
    )(*grads, *theirs)


def _chip_exchange(parts):
    def copies_of(src, outs, send_sems, recv_sems):
        x, y, c = _place()
        return [pltpu.make_async_remote_copy(
            src_ref=src[a].at[2 * chip[0] + chip[1]], dst_ref=outs[a].at[k], send_sem=send_sems.at[3 * a + k],
            recv_sem=recv_sems.at[3 * a + k], device_id=(*chip, c), device_id_type=MESH)
            for a in range(len(src)) for k, chip in enumerate(_other_chips(x, y))]

    return _simple_exchange(parts, [jax.ShapeDtypeStruct((NCHIP - 1,) + p.shape[1:], p.dtype) for p in parts], copies_of)


def _chip_sum(parts, received):
    n = len(parts)
    steps = 4

    def body(*refs):
        chip = 2 * lax.axis_index("x") + lax.axis_index("y")
        for a in range(n):
            p, r = refs[a], refs[n + a]
            own = jnp.where(chip == 0, p[0], jnp.where(chip == 1, p[1], jnp.where(chip == 2, p[2], p[3])))
            refs[2 * n + a][...] = ((own.astype(F32) + r[0].astype(F32)) + r[1].astype(F32)) + r[2].astype(F32)

    def specs(arrs):
        return [pl.BlockSpec((g.shape[0], g.shape[1] // steps, g.shape[2]), lambda i: (0, i, 0)) for g in arrs]

    out_specs = [pl.BlockSpec((g.shape[1] // steps, g.shape[2]), lambda i: (i, 0)) for g in parts]
    return pl.pallas_call(
        body, name="grads_chip_sum", grid=(steps,), in_specs=specs(parts) + specs(received), out_specs=out_specs,
        out_shape=[jax.ShapeDtypeStruct(g.shape[1:], F32) for g in parts], compiler_params=_cparams(),
    )(*parts, *received)


def _pair_share(halves):
    def copies_of(src, outs, send_sems, recv_sems):
        x, y, c = _place()
        return [pltpu.make_async_remote_copy(
            src_ref=src[a], dst_ref=outs[a], send_sem=send_sems.at[a], recv_sem=recv_sems.at[a],
            device_id=(x, y, 1 - c), device_id_type=MESH) for a in range(len(src))]

    return _simple_exchange(halves, [jax.ShapeDtypeStruct(h.shape, F32) for h in halves], copies_of)


def _adamw_math(w, g, m, v):
    nm = ADAM_B1 * m + (1.0 - ADAM_B1) * g
    nv = ADAM_B2 * v + (1.0 - ADAM_B2) * jnp.square(g)
    m_hat = nm / (1.0 - ADAM_B1 ** ADAM_STEP)
    v_hat = nv / (1.0 - ADAM_B2 ** ADAM_STEP)
    return -ADAM_LR * (m_hat / (jnp.sqrt(v_hat) + ADAM_EPS) + ADAM_WD * w), nm, nv


def _adamw_big(ws, g_mine, g_theirs, ms, vs):
    n = len(ws)
    steps = 8

    def body(*refs):
        own_half = (pl.program_id(0) // (steps // 2)) == lax.axis_index("c")
        for a in range(n):
            w = refs[a][...]
            g = jnp.where(own_half, refs[n + a][...], refs[2 * n + a][...])[:, :w.shape[1]]
            d, nm, nv = _adamw_math(w, g, refs[3 * n + a][...], refs[4 * n + a][...])
            refs[5 * n + a][...] = g
            refs[6 * n + a][...] = d
            refs[7 * n + a][...] = nm
            refs[8 * n + a][...] = nv

    specs = [pl.BlockSpec((w.shape[0] // steps, w.shape[1]), lambda i: (i, 0)) for w in ws]
    half_specs = [pl.BlockSpec((g.shape[0] // (steps // 2), g.shape[1]), lambda i: (i % (steps // 2), 0)) for g in g_mine]
    shapes = [jax.ShapeDtypeStruct(w.shape, F32) for w in ws]
    res = pl.pallas_call(
        body, name="adamw_big", grid=(steps,), in_specs=specs + half_specs * 2 + specs * 2, out_specs=specs * 4,
        out_shape=shapes * 4, compiler_params=_cparams(),
    )(*ws, *g_mine, *g_theirs, *ms, *vs)
    return res[:n], res[n:2 * n], res[2 * n:3 * n], res[3 * n:]


NORM_NAMES = ("pre_mix_norm", "post_mix_norm", "pre_mlp_norm", "post_mlp_norm")
SMALL_NAMES = NORM_NAMES + ("gdn_conv_w", "fox_f_bias", "gdn_dt_bias", "gdn_a_log", "fox_out_norm", "gdn_out_norm")
CONV_COLS = 3 * DGDN // NCHIP


def _small_allreduce(d_norms, d_conv, sums, d_fox_norm, d_gdn_norm):
    def body(*refs):
        dn_ref, dconv_ref, sums_ref, dfn_ref, dgn_ref = refs[:5]
        outs = refs[5:10]
        g_norms, g_conv, g_sums, g_fn, g_gn, send_sems, recv_sems, local_sem = refs[10:]
        x, y, c = _place()
        me = 4 * x + 2 * y + c
        g_norms[me] = dn_ref[...]
        g_sums[me] = sums_ref[...]
        g_fn[me] = dfn_ref[...]
        g_gn[me] = dgn_ref[...]

        def conv_cols(chip_index):
            return dconv_ref.at[:, pl.ds(pl.multiple_of(chip_index * CONV_COLS, LANES), CONV_COLS)]

        own = pltpu.make_async_copy(conv_cols(2 * x + y), g_conv.at[me], local_sem)
        own.start()
        copies = []
        for k in range(1, NDEV):
            px, py, pc = x ^ ((k >> 2) & 1), y ^ ((k >> 1) & 1), c ^ (k & 1)
            pairs = [(dn_ref, g_norms), (conv_cols(2 * px + py), g_conv), (sums_ref, g_sums), (dfn_ref, g_fn), (dgn_ref, g_gn)]
            for a, (src, dst) in enumerate(pairs):
                cp = pltpu.make_async_remote_copy(
                    src_ref=src, dst_ref=dst.at[me], send_sem=send_sems.at[5 * (k - 1) + a],
                    recv_sem=recv_sems.at[5 * (k - 1) + a], device_id=(px, py, pc), device_id_type=MESH)
                cp.start()
                copies.append(cp)
        own.wait()
        for cp in copies:
            cp.wait()

        def total(buf):
            acc = buf[0]
            for i in range(1, NDEV):
                acc = acc + buf[i]
            return acc

        for out, buf in zip(outs, (g_norms, g_conv, g_sums, g_fn, g_gn)):
            out[...] = total(buf)

    n_sem = 5 * (NDEV - 1)
    shapes = [(4, D), (CONV_K, CONV_COLS), (8, LANES), (1, LANES), (1, LANES)]
    return pl.pallas_call(
        body, name="small_allreduce", in_specs=[VMEM] * 5, out_specs=[VMEM] * 5,
        out_shape=[jax.ShapeDtypeStruct(s, F32) for s in shapes],
        scratch_shapes=[pltpu.VMEM((NDEV,) + s, F32) for s in shapes]
        + [pltpu.SemaphoreType.DMA((n_sem,)), pltpu.SemaphoreType.DMA((n_sem,)), pltpu.SemaphoreType.DMA],
        compiler_params=_cparams(),
    )(d_norms, d_conv, sums, d_fox_norm, d_gdn_norm)


def _small_adamw(totals, ws, ms, vs):
    n = len(SMALL_NAMES)

    def body(*refs):
        t_norms, t_conv, t_sums, t_fn, t_gn = [r[...] for r in refs[:5]]
        w_refs, m_refs, v_refs = refs[5:5 + n], refs[5 + n:5 + 2 * n], refs[5 + 2 * n:5 + 3 * n]
        outs = refs[5 + 3 * n:]
        grads = [t_norms[i:i + 1, :] for i in range(4)] + [
            t_conv, t_sums[0:1, 0:NFH], t_sums[1:2, 0:NGH], t_sums[2:3, 0:NGH], t_fn[:, 0:FHD], t_gn]
        for a in range(n):
            d, nm, nv = _adamw_math(w_refs[a][...], grads[a], m_refs[a][...], v_refs[a][...])
            outs[a][...] = grads[a]
            outs[n + a][...] = d
            outs[2 * n + a][...] = nm
            outs[3 * n + a][...] = nv

    def whole(arr):
        return pl.BlockSpec(arr.shape, lambda i: (0, 0))

    res = pl.pallas_call(
        body, name="small_adamw", grid=(1,), in_specs=[whole(t) for t in totals] + [whole(w) for w in ws] * 3,
        out_specs=[whole(w) for w in ws] * 4, out_shape=[jax.ShapeDtypeStruct(w.shape, F32) for w in ws] * 4,
        compiler_params=_cparams(),
    )(*totals, *ws, *ms, *vs)
    return res[:n], res[n:2 * n], res[2 * n:3 * n], res[3 * n:]


def _to_padded_cols(w):
    pieces = [w[:, part * DFOX + hp * LANES:part * DFOX + (hp + 1) * LANES] for hp in range(NPAIR) for part in range(3)]
    pieces += [w[:, 1544:3080], w[:, 3088:3600], w[:, 1536:1544], w[:, 3080:3088],
               jnp.zeros((w.shape[0], 2 * LANES - 16), w.dtype)]
    return jnp.concatenate(pieces, axis=1)


def _from_padded_cols(w):
    c0 = BLK_SMALL * LANES
    fox = [w[:, (3 * hp + part) * LANES:(3 * hp + part + 1) * LANES] for part in range(3) for hp in range(NPAIR)]
    return jnp.concatenate(fox + [w[:, c0:c0 + 8], w[:, BLK_GDN * LANES:BLK_GZ * LANES], w[:, c0 + 8:c0 + 16],
                                  w[:, BLK_GZ * LANES:BLK_SMALL * LANES]], axis=1)


def _local_step(x, target, win_p, late_weights, reduce_mlp, pre_mix_norm, fox_f_bias, fox_out_norm, conv_w, gdn_a_log,
                gdn_dt_bias, gdn_out_norm, post_mix_norm, pre_mlp_norm, post_mlp_norm):
    bias_vec = jnp.zeros((1, LANES), F32).at[0, 0:NFH].set(fox_f_bias).at[0, LANE_G:LANE_G + NGH].set(gdn_dt_bias)
    alog_vec = jnp.zeros((1, LANES), F32).at[0, LANE_G:LANE_G + NGH].set(gdn_a_log)
    w2 = jnp.concatenate([fox_out_norm, fox_out_norm], axis=1)

    h = _pre_norm(x, pre_mix_norm)
    proj = _matmul(h, win_p, tm=2048, tn=768, tk=1024, name="mm_proj")
    gates = _gates(proj, bias_vec, alog_vec)
    mix, fox_o, lse, late = _fox_fwd(proj, gates, w2, exchange=late_weights[0])
    wout, wup3, wdown = late_weights[1](late)
    qkv = _gdn_pre(proj, conv_w)
    u, w, qd, kd, a_intra, gl, t_inv = _gdn_prep(qkv, gates)
    mix, gdn_raw, states = _gdn_scan(u, w, qd, kd, a_intra, gl, proj, gdn_out_norm, mix)
    mixed = _matmul(mix, wout, tm=2048, tk=1024, name="mm_out")
    x1, h2 = _post_mix(x, mixed, post_mix_norm, pre_mlp_norm)

    def relu2(acc):
        r = jnp.maximum(acc, 0.0)
        return acc, r * r

    up, act = _matmul(h2, wup3, b3=True, tm=1024, tn=1024, tk=1024, out_dtypes=(F32, BF16), epilogue=relu2, name="mm_up")
    y = _matmul(act, wdown, tm=2048, tk=1024, name="mm_down")
    dx2, dy, d_post_mlp, loss_row = _loss_head(x1, y, post_mlp_norm, target)

    dwdown = _matmul(act, dy, ta=True, tm=1024, tn=1024, tk=2048, out_dtypes=(BF16,), name="mm_dwdown")

    def relu2_bwd(acc, upv):
        return (acc * 2.0 * jnp.maximum(upv, 0.0),)

    dup = _matmul(dy, wdown, tb=True, tm=1024, tn=1024, tk=1024, out_dtypes=(BF16,), extra=(up,), epilogue=relu2_bwd,
                  name="mm_dact")
    dwup3 = _matmul(h2, dup, ta=True, tm=1024, tn=1024, tk=2048, out_dtypes=(BF16,), o3=True, name="mm_dwup")
    dh2 = _matmul(dup, wup3, tb=True, b3=True, tm=2048, tk=1024, name="mm_dh2")
    dx1, dmixed, d_pre_mlp, d_post_mix = _mid_bwd(dh2, x1, pre_mlp_norm, dx2, mixed, post_mix_norm)
    dwout = _matmul(mix, dmixed, ta=True, tm=1024, tn=1024, tk=2048, out_dtypes=(BF16,), name="mm_dwout")
    dmix = _matmul(dmixed, wout, tb=True, tm=2048, tk=1024, name="mm_dmix")

    dfox, delta, d_fox_norm = _fox_norm_bwd(fox_o, dmix, w2)
    dproj, dcum_fox, reduced_mlp = _fox_bwd(proj, dfox, gates, lse, delta, exchange=reduce_mlp(dwup3, dwdown))
    dproj, du, dw, dqd, dkd, da, dgl, d_gdn_norm = _gdn_scan_bwd(dmix, gdn_raw, proj, gdn_out_norm, u, w, qd, kd,
                                                                 a_intra, gl, states, dproj)
    dqkv, dgates_gdn = _gdn_prep_bwd(qkv, gates, t_inv, du, dw, dqd, dkd, da, dgl)
    dproj, d_conv = _gdn_pre_bwd(proj, conv_w, dqkv, dproj)
    dproj, sums = _gates_bwd(proj, bias_vec, alog_vec, dgates_gdn, dcum_fox, dproj)

    dwin_p = _matmul(h, dproj, ta=True, tm=1024, tn=1280, tk=2048, out_dtypes=(BF16,), name="mm_dwin")
    dh = _matmul(dproj, win_p, tb=True, tm=2048, tk=1280, name="mm_dh")
    grad_x, d_pre_mix = _pre_norm_bwd(dh, x, pre_mix_norm, dx1)

    d_norms = jnp.concatenate([d_pre_mix, d_post_mix, d_pre_mlp, d_post_mlp], axis=0)
    return (loss_row[0, 0], grad_x, (dwin_p, dwout, dwup3, dwdown), (d_norms, d_conv, sums, d_fox_norm, d_gdn_norm),
            reduced_mlp)


def kernel(x, pre_mix_norm, w_in, fox_f_bias, fox_out_norm, gdn_conv_w, gdn_a_log, gdn_dt_bias, gdn_out_norm, w_out, post_mix_norm, pre_mlp_norm, w_up, w_down, post_mlp_norm, loss_target, m_pre_mix_norm, m_w_in, m_fox_f_bias, m_fox_out_norm, m_gdn_conv_w, m_gdn_a_log, m_gdn_dt_bias, m_gdn_out_norm, m_w_out, m_post_mix_norm, m_pre_mlp_norm, m_w_up, m_w_down, m_post_mlp_norm, v_pre_mix_norm, v_w_in, v_fox_f_bias, v_fox_out_norm, v_gdn_conv_w, v_gdn_a_log, v_gdn_dt_bias, v_gdn_out_norm, v_w_out, v_post_mix_norm, v_pre_mlp_norm, v_w_up, v_w_down, v_post_mlp_norm):
    weights = dict(pre_mix_norm=pre_mix_norm, w_in=w_in, fox_f_bias=fox_f_bias, fox_out_norm=fox_out_norm, gdn_conv_w=gdn_conv_w,
                   gdn_a_log=gdn_a_log, gdn_dt_bias=gdn_dt_bias, gdn_out_norm=gdn_out_norm, w_out=w_out, post_mix_norm=post_mix_norm,
                   pre_mlp_norm=pre_mlp_norm, w_up=w_up, w_down=w_down, post_mlp_norm=post_mlp_norm)
    m_in = dict(pre_mix_norm=m_pre_mix_norm, w_in=m_w_in, fox_f_bias=m_fox_f_bias, fox_out_norm=m_fox_out_norm, gdn_conv_w=m_gdn_conv_w,
                gdn_a_log=m_gdn_a_log, gdn_dt_bias=m_gdn_dt_bias, gdn_out_norm=m_gdn_out_norm, w_out=m_w_out, post_mix_norm=m_post_mix_norm,
                pre_mlp_norm=m_pre_mlp_norm, w_up=m_w_up, w_down=m_w_down, post_mlp_norm=m_post_mlp_norm)
    v_in = dict(pre_mix_norm=v_pre_mix_norm, w_in=v_w_in, fox_f_bias=v_fox_f_bias, fox_out_norm=v_fox_out_norm, gdn_conv_w=v_gdn_conv_w,
                gdn_a_log=v_gdn_a_log, gdn_dt_bias=v_gdn_dt_bias, gdn_out_norm=v_gdn_out_norm, w_out=v_w_out, post_mix_norm=v_post_mix_norm,
                pre_mlp_norm=v_pre_mlp_norm, w_up=v_w_up, w_down=v_w_down, post_mlp_norm=v_post_mlp_norm)
    order_w = ("pre_mix_norm", "w_in", "fox_f_bias", "fox_out_norm", "gdn_conv_w", "gdn_a_log", "gdn_dt_bias", "gdn_out_norm", "w_out",
               "post_mix_norm", "pre_mlp_norm", "w_up", "w_down", "post_mlp_norm")
    big = ("w_in", "w_out", "w_up", "w_down")

    def row(v):
        return v if v.ndim == 2 else v.reshape(1, -1)

    cw = DPROJ // NCHIP
    win_shard = jnp.pad(w_in.astype(BF16), ((0, 0), (0, D - cw)))
    win_g, conv_g = _run_exchange(_allgather_exchange([win_shard], whole=[gdn_conv_w]), "weights_allgather_in")
    win_p = _to_padded_cols(_with_own(win_g, win_shard)[:, :, :cw].transpose(1, 0, 2).reshape(D, DPROJ))
    conv_full = _with_own(conv_g, gdn_conv_w).transpose(1, 0, 2).reshape(CONV_K, 3 * DGDN)
    late_shards = [weights[n].astype(BF16) for n in big[1:]]

    def resolve_late(gathered):
        wout_g, wup3, wdown_g = [_with_own(g, own) for g, own in zip(gathered, late_shards)]
        return wout_g.reshape(D, D), wup3, wdown_g.reshape(DFF, D)

    mlp_sums = []

    def reduce_mlp(dwup3, dwdown):
        blocks = [dwup3, dwdown.reshape(NCHIP, DFF // NCHIP, D)]
        theirs = _run_exchange(_pair_exchange(blocks), "grads_pair_exchange_mlp")
        mlp_sums.extend(_pair_sum(blocks, theirs, "grads_pair_sum_mlp"))
        return _chip_exchange(mlp_sums)

    loss_local, grad_x, (dwin_p, dwout, _, _), small, mlp_received = _local_step(
        x[0], loss_target[0], win_p, (_allgather_exchange(late_shards), resolve_late), reduce_mlp, row(pre_mix_norm),
        fox_f_bias, row(fox_out_norm), conv_full, gdn_a_log, gdn_dt_bias, row(gdn_out_norm), row(post_mix_norm),
        row(pre_mlp_norm), row(post_mlp_norm))
    loss = lax.psum(loss_local, ("x", "y", "c"))

    dwin3 = jnp.pad(_from_padded_cols(dwin_p).reshape(D, NCHIP, cw).transpose(1, 0, 2), ((0, 0), (0, 0), (0, D - cw)))
    blocks = [dwin3, dwout.reshape(NCHIP, D // NCHIP, D)]
    pair_sums = _pair_sum(blocks, _run_exchange(_pair_exchange(blocks), "grads_pair_exchange"), "grads_pair_sum")
    received = _run_exchange(_chip_exchange(pair_sums), "grads_chip_exchange")
    g_mine = _chip_sum(list(pair_sums) + mlp_sums, list(received) + list(mlp_received))
    g_theirs = _run_exchange(_pair_share(g_mine), "grads_pair_share")

    g_small, d_small, nm_small, nv_small = _small_adamw(
        _small_allreduce(*small), [row(weights[n]) for n in SMALL_NAMES], [row(m_in[n]) for n in SMALL_NAMES],
        [row(v_in[n]) for n in SMALL_NAMES])

    g_big, d_big, nm_big, nv_big = _adamw_big([weights[n] for n in big], g_mine, g_theirs, [m_in[n] for n in big],
                                              [v_in[n] for n in big])

    grads, delta, new_m, new_v = {}, {}, {}, {}
    for i, n in enumerate(big):
        grads[n], delta[n], new_m[n], new_v[n] = g_big[i], d_big[i], nm_big[i], nv_big[i]
    for i, n in enumerate(SMALL_NAMES):
        shape = weights[n].shape
        grads[n], delta[n], new_m[n], new_v[n] = (g_small[i].reshape(shape), d_small[i].reshape(shape),
                                                  nm_small[i].reshape(shape), nv_small[i].reshape(shape))
    return (loss, grad_x[None], *[grads[n] for n in order_w], *[delta[n] for n in order_w], *[new_m[n] for n in order_w],
            *[new_v[n] for n in order_w])
```

```python
import jax
import jax.numpy as jnp
from jax import lax
from jax.experimental import pallas as pl
from jax.experimental.pallas import tpu as pltpu

F32 = jnp.float32
BF16 = jnp.bfloat16
MESH = pl.DeviceIdType.MESH

S = 2048
D = 1024
NFH, FHD = 8, 64
NPAIR = NFH // 2
NGH, GHD = 4, 128
DFOX = NFH * FHD
DGDN = NGH * GHD
CHUNK = 64
NCH = S // CHUNK
CONV_K = 4
DFF = 4 * D
EPS = 1e-6
DPROJ = 3600
LANES = 128
DPROJ_PAD = 3840
BLK_GDN = 12
BLK_GZ = 24
BLK_SMALL = 28
NCHIP = 4
NDEV = 8
VMEM_LIMIT = 56 * 1024 * 1024

ADAM_LR = 0.001
ADAM_B1 = 0.9
ADAM_B2 = 0.999
ADAM_EPS = 1e-08
ADAM_WD = 0.01
ADAM_STEP = 10


def _cparams(**kw):
    return pltpu.CompilerParams(vmem_limit_bytes=VMEM_LIMIT, **kw)


def _dn(ca, cb):
    return (((ca,), (cb,)), ((), ()))


def _dot(a, b, ca=1, cb=0):
    return lax.dot_general(a.astype(BF16), b.astype(BF16), _dn(ca, cb), preferred_element_type=F32)


def _hdot(a, b, ca=1, cb=0):
    return lax.dot_general(a.astype(F32), b.astype(F32), _dn(ca, cb), precision=lax.Precision.HIGHEST,
                           preferred_element_type=F32)


@jax.custom_vjp
def _mm_nn(a, b):
    return _dot(a, b, 1, 0)


def _mm_nn_fwd(a, b):
    return _dot(a, b, 1, 0), (a, b)


def _mm_nn_bwd(res, g):
    a, b = res
    return _dot(g, b, 1, 1), _dot(a, g, 0, 0)


_mm_nn.defvjp(_mm_nn_fwd, _mm_nn_bwd)


@jax.custom_vjp
def _mm_nt(a, b):
    return _dot(a, b, 1, 1)


def _mm_nt_fwd(a, b):
    return _dot(a, b, 1, 1), (a, b)


def _mm_nt_bwd(res, g):
    a, b = res
    return _dot(g, b, 1, 0), _dot(g, a, 0, 0)


_mm_nt.defvjp(_mm_nt_fwd, _mm_nt_bwd)


@jax.custom_vjp
def _saved_inverse(m, t_inv):
    del m
    return t_inv


def _saved_inverse_fwd(m, t_inv):
    del m
    return t_inv, t_inv


def _saved_inverse_bwd(t_inv, g):
    return -_hdot(_hdot(t_inv, g, 0, 0), t_inv, 1, 1), jnp.zeros_like(t_inv)


_saved_inverse.defvjp(_saved_inverse_fwd, _saved_inverse_bwd)


def _sigmoid(z):
    return 1.0 / (1.0 + jnp.exp(-z))


def _softplus(z):
    return jnp.maximum(z, 0.0) + jnp.log(1.0 + jnp.exp(-jnp.abs(z)))


def _silu(z):
    return z * _sigmoid(z)


def _rms_scale(x):
    return lax.rsqrt(jnp.mean(x * x, axis=-1, keepdims=True) + EPS)


def _rms_bwd(x, w, g):
    r = _rms_scale(x)
    gw = g * w
    dx = r * gw - x * (r * r * r) * jnp.mean(gw * x, axis=-1, keepdims=True)
    return dx, g * x * r


def _matmul(a, b, *, name, ta=False, tb=False, tm=512, tn=512, tk=512, out_dtypes=(F32,), b3=False, o3=False,
            extra=(), epilogue=None):
    m, k = (a.shape[1], a.shape[0]) if ta else a.shape
    if b3:
        n = b.shape[1] if tb else b.shape[0] * b.shape[2]
        kb = b.shape[0] * b.shape[2] if tb else b.shape[1]
    else:
        n, kb = (b.shape[0], b.shape[1]) if tb else (b.shape[1], b.shape[0])
    assert kb == k, (name, kb, k)
    tm, tn, tk = min(tm, m), min(tn, n), min(tk, k)
    assert m % tm == 0 and n % tn == 0 and k % tk == 0, (name, m, n, k, tm, tn, tk)
    nk = k // tk
    n_extra = len(extra)
    n_out = len(out_dtypes)

    def body(*refs):
        a_ref, b_ref = refs[0], refs[1]
        extra_refs = refs[2:2 + n_extra]
        out_refs = refs[2 + n_extra:2 + n_extra + n_out]

        def finish(acc):
            outs = (acc,) if epilogue is None else epilogue(acc, *[r[...] for r in extra_refs])
            for o_ref, val in zip(out_refs, outs):
                o_ref[...] = val.astype(o_ref.dtype)

        part = _dot(a_ref[...], b_ref[...], 0 if ta else 1, 1 if tb else 0)
        if nk == 1:
            finish(part)
            return
        acc_ref = refs[-1]
        kk = pl.program_id(2)

        @pl.when(kk == 0)
        def _():
            acc_ref[...] = part

        @pl.when(kk > 0)
        def _():
            acc_ref[...] += part

        @pl.when(kk == nk - 1)
        def _():
            finish(acc_ref[...])

    a_spec = pl.BlockSpec((tk, tm), lambda i, j, kk: (kk, i)) if ta else pl.BlockSpec((tm, tk), lambda i, j, kk: (i, kk))
    if b3 and tb:
        assert b.shape[2] == tk
        b_spec = pl.BlockSpec((None, tn, tk), lambda i, j, kk: (kk, j, 0))
    elif b3:
        assert b.shape[2] == tn
        b_spec = pl.BlockSpec((None, tk, tn), lambda i, j, kk: (j, kk, 0))
    elif tb:
        b_spec = pl.BlockSpec((tn, tk), lambda i, j, kk: (j, kk))
    else:
        b_spec = pl.BlockSpec((tk, tn), lambda i, j, kk: (kk, j))
    tile = pl.BlockSpec((tm, tn), lambda i, j, kk: (i, j))
    out_specs = [tile] * n_out
    out_shape = [jax.ShapeDtypeStruct((m, n), dt) for dt in out_dtypes]
    if o3:
        out_specs[0] = pl.BlockSpec((None, tm, tn), lambda i, j, kk: (j, i, 0))
        out_shape[0] = jax.ShapeDtypeStruct((n // tn, m, tn), out_dtypes[0])
    res = pl.pallas_call(
        body, name=name, grid=(m // tm, n // tn, nk),
        in_specs=[a_spec, b_spec] + [tile] * n_extra, out_specs=out_specs, out_shape=out_shape,
        scratch_shapes=[pltpu.VMEM((tm, tn), F32)] if nk > 1 else [],
        compiler_params=_cparams(dimension_semantics=("parallel", "parallel", "arbitrary")),
    )(a, b, *extra)
    return res[0] if n_out == 1 else res


TR = 256


def _row_spec(cols):
    return pl.BlockSpec((TR, cols), lambda i: (i, 0))


def _vec_spec(cols):
    return pl.BlockSpec((1, cols), lambda i: (0, 0))


def _pre_norm(x, w):
    def body(x_ref, w_ref, h_ref):
        xv = x_ref[...]
        h_ref[...] = (xv * _rms_scale(xv) * w_ref[...]).astype(BF16)

    return pl.pallas_call(
        body, name="pre_norm", grid=(S // TR,), in_specs=[_row_spec(D), _vec_spec(D)], out_specs=_row_spec(D),
        out_shape=jax.ShapeDtypeStruct((S, D), BF16), compiler_params=_cparams(),
    )(x, w)


def _post_mix(x, mixed, w_post, w_pre_mlp):
    def body(x_ref, m_ref, wp_ref, wm_ref, x1_ref, h2_ref):
        mv = m_ref[...]
        x1 = x_ref[...] + mv * _rms_scale(mv) * wp_ref[...]
        x1_ref[...] = x1
        h2_ref[...] = (x1 * _rms_scale(x1) * wm_ref[...]).astype(BF16)

    return pl.pallas_call(
        body, name="post_mix", grid=(S // TR,),
        in_specs=[_row_spec(D), _row_spec(D), _vec_spec(D), _vec_spec(D)], out_specs=[_row_spec(D), _row_spec(D)],
        out_shape=[jax.ShapeDtypeStruct((S, D), F32), jax.ShapeDtypeStruct((S, D), BF16)], compiler_params=_cparams(),
    )(x, mixed, w_post, w_pre_mlp)


def _loss_head(x1, y, w_post_mlp, target):
    def body(x1_ref, y_ref, w_ref, t_ref, dx2_ref, dy_ref, dw_ref, loss_ref):
        i = pl.program_id(0)
        yv = y_ref[...]
        w = w_ref[...]
        x2 = x1_ref[...] + yv * _rms_scale(yv) * w
        err = x2 - t_ref[...]
        dx2 = err * (1.0 / D)
        dx2_ref[...] = dx2
        dy, dwt = _rms_bwd(yv, w, dx2)
        dy_ref[...] = dy.astype(BF16)

        @pl.when(i == 0)
        def _():
            dw_ref[...] = jnp.zeros_like(dw_ref)
            loss_ref[...] = jnp.zeros_like(loss_ref)

        dw_ref[...] += jnp.sum(dwt, axis=0, keepdims=True)
        part = 0.5 * jnp.sum(jnp.mean(err * err, axis=-1, keepdims=True), axis=0, keepdims=True)
        loss_ref[...] += jnp.broadcast_to(part, loss_ref.shape)

    return pl.pallas_call(
        body, name="loss_head", grid=(S // TR,),
        in_specs=[_row_spec(D), _row_spec(D), _vec_spec(D), _row_spec(D)],
        out_specs=[_row_spec(D), _row_spec(D), _vec_spec(D), _vec_spec(LANES)],
        out_shape=[jax.ShapeDtypeStruct((S, D), F32), jax.ShapeDtypeStruct((S, D), BF16),
                   jax.ShapeDtypeStruct((1, D), F32), jax.ShapeDtypeStruct((1, LANES), F32)],
        compiler_params=_cparams(),
    )(x1, y, w_post_mlp, target)


def _mid_bwd(dh2, x1, w_pre_mlp, dx2, mixed, w_post):
    def body(dh2_ref, x1_ref, wm_ref, dx2_ref, m_ref, wp_ref, dx1_ref, dm_ref, dwm_ref, dwp_ref):
        i = pl.program_id(0)
        dxa, dwm = _rms_bwd(x1_ref[...], wm_ref[...], dh2_ref[...])
        dx1 = dx2_ref[...] + dxa
        dx1_ref[...] = dx1
        dm, dwp = _rms_bwd(m_ref[...], wp_ref[...], dx1)
        dm_ref[...] = dm.astype(BF16)

        @pl.when(i == 0)
        def _():
            dwm_ref[...] = jnp.zeros_like(dwm_ref)
            dwp_ref[...] = jnp.zeros_like(dwp_ref)

        dwm_ref[...] += jnp.sum(dwm, axis=0, keepdims=True)
        dwp_ref[...] += jnp.sum(dwp, axis=0, keepdims=True)

    return pl.pallas_call(
        body, name="mid_bwd", grid=(S // TR,),
        in_specs=[_row_spec(D), _row_spec(D), _vec_spec(D), _row_spec(D), _row_spec(D), _vec_spec(D)],
        out_specs=[_row_spec(D), _row_spec(D), _vec_spec(D), _vec_spec(D)],
        out_shape=[jax.ShapeDtypeStruct((S, D), F32), jax.ShapeDtypeStruct((S, D), BF16),
                   jax.ShapeDtypeStruct((1, D), F32), jax.ShapeDtypeStruct((1, D), F32)],
        compiler_params=_cparams(),
    )(dh2, x1, w_pre_mlp, dx2, mixed, w_post)


def _pre_norm_bwd(dh, x, w, dx1):
    def body(dh_ref, x_ref, w_ref, dx1_ref, dx_ref, dw_ref):
        i = pl.program_id(0)
        dxa, dwt = _rms_bwd(x_ref[...], w_ref[...], dh_ref[...])
        dx_ref[...] = dx1_ref[...] + dxa

        @pl.when(i == 0)
        def _():
            dw_ref[...] = jnp.zeros_like(dw_ref)

        dw_ref[...] += jnp.sum(dwt, axis=0, keepdims=True)

    return pl.pallas_call(
        body, name="pre_norm_bwd", grid=(S // TR,),
        in_specs=[_row_spec(D), _row_spec(D), _vec_spec(D), _row_spec(D)], out_specs=[_row_spec(D), _vec_spec(D)],
        out_shape=[jax.ShapeDtypeStruct((S, D), F32), jax.ShapeDtypeStruct((1, D), F32)], compiler_params=_cparams(),
    )(dh, x, w, dx1)


BQ = 256
NQ = S // BQ
LANE_BETA, LANE_G = 8, 12


def _gate_lanes(shape):
    lane = lax.broadcasted_iota(jnp.int32, shape, 1)
    return lane < LANE_BETA, (lane >= LANE_BETA) & (lane < LANE_G), (lane >= LANE_G) & (lane < LANE_G + NGH)


def _gates(proj, bias_vec, alog_vec):
    def body(s_ref, b_ref, a_ref, o_ref, carry_ref):
        i = pl.program_id(0)

        @pl.when(i == 0)
        def _():
            carry_ref[...] = jnp.zeros_like(carry_ref)

        z = s_ref[...] + b_ref[...]
        tail = jnp.log(1.0 + jnp.exp(-jnp.abs(z)))
        sp = jnp.maximum(z, 0.0) + tail
        lf = jnp.minimum(z, 0.0) - tail
        r = lax.broadcasted_iota(jnp.int32, (BQ, BQ), 0)
        c = lax.broadcasted_iota(jnp.int32, (BQ, BQ), 1)
        tri = (c <= r).astype(F32)
        cum = _hdot(tri, lf) + carry_ref[...]
        carry_ref[...] = cum[BQ - 1:BQ, :]
        is_fox, is_beta, is_g = _gate_lanes(z.shape)
        o_ref[...] = jnp.where(is_fox, cum, jnp.where(is_beta, _sigmoid(z), jnp.where(is_g, -jnp.exp(a_ref[...]) * sp, 0.0)))

    return pl.pallas_call(
        body, name="gates", grid=(NQ,),
        in_specs=[pl.BlockSpec((BQ, LANES), lambda i: (i, BLK_SMALL)), _vec_spec(LANES), _vec_spec(LANES)],
        out_specs=pl.BlockSpec((BQ, LANES), lambda i: (i, 0)), out_shape=jax.ShapeDtypeStruct((S, LANES), F32),
        scratch_shapes=[pltpu.VMEM((1, LANES), F32)], compiler_params=_cparams(),
    )(proj, bias_vec, alog_vec)


def _gates_bwd(proj, bias_vec, alog_vec, dgates_gdn, dcum_fox, dproj):
    def body(s_ref, b_ref, a_ref, dg_ref, dc_ref, dproj_in, dproj_ref, red_ref, carry_ref):
        del dproj_in
        i = pl.program_id(0)

        @pl.when(i == 0)
        def _():
            carry_ref[...] = jnp.zeros_like(carry_ref)
            red_ref[...] = jnp.zeros_like(red_ref)

        z = s_ref[...] + b_ref[...]
        dg = dg_ref[...] + dc_ref[...]
        r = lax.broadcasted_iota(jnp.int32, (BQ, BQ), 0)
        c = lax.broadcasted_iota(jnp.int32, (BQ, BQ), 1)
        upper = (c >= r).astype(F32)
        dlf = _hdot(upper, dg) + carry_ref[...]
        carry_ref[...] = dlf[0:1, :]
        sig = _sigmoid(z)
        g_scale = -jnp.exp(a_ref[...])
        is_fox, is_beta, is_g = _gate_lanes(z.shape)
        ds = jnp.where(is_fox, dlf * (1.0 - sig), jnp.where(is_beta, dg * sig * (1.0 - sig), jnp.where(is_g, dg * g_scale * sig, 0.0)))
        dproj_ref[:, 0:LANES] = ds.astype(BF16)
        dproj_ref[:, LANES:2 * LANES] = jnp.zeros((BQ, LANES), BF16)
        dalog = jnp.where(is_g, dg * g_scale * _softplus(z), 0.0)
        sums = jnp.sum(ds, axis=0, keepdims=True)
        red_ref[0:1, :] += jnp.where(is_fox[0:1], sums, 0.0)
        red_ref[1:2, :] += pltpu.roll(jnp.where(is_g[0:1], sums, 0.0), LANES - LANE_G, 1)
        red_ref[2:3, :] += pltpu.roll(jnp.sum(dalog, axis=0, keepdims=True), LANES - LANE_G, 1)

    blk = pl.BlockSpec((BQ, LANES), lambda i: (NQ - 1 - i, 0))
    return pl.pallas_call(
        body, name="gates_bwd", grid=(NQ,),
        in_specs=[pl.BlockSpec((BQ, LANES), lambda i: (NQ - 1 - i, BLK_SMALL)), _vec_spec(LANES), _vec_spec(LANES), blk, blk,
                  pl.BlockSpec(memory_space=pl.ANY)],
        out_specs=[pl.BlockSpec((BQ, 2 * LANES), lambda i: (NQ - 1 - i, BLK_SMALL // 2)), pl.BlockSpec((8, LANES), lambda i: (0, 0))],
        out_shape=[jax.ShapeDtypeStruct((S, DPROJ_PAD), BF16), jax.ShapeDtypeStruct((8, LANES), F32)],
        input_output_aliases={5: 0},
        scratch_shapes=[pltpu.VMEM((1, LANES), F32)], compiler_params=_cparams(),
    )(proj, bias_vec, alog_vec, dgates_gdn, dcum_fox, dproj)


FOX_SCALE = FHD ** -0.5
FOX_PAIRS = 2
FOX_PAIRS_BWD = 2


def _head_mask(e):
    lane = lax.broadcasted_iota(jnp.int32, (1, LANES), 1)
    return (lane >= e * FHD) & (lane < (e + 1) * FHD)


def _lane_col(vals, index):
    lane = lax.broadcasted_iota(jnp.int32, vals.shape, 1)
    return jnp.sum(jnp.where(lane == index, vals, 0.0), axis=1, keepdims=True)


def _sublane_row(vals, index):
    row = lax.broadcasted_iota(jnp.int32, vals.shape, 0)
    return jnp.sum(jnp.where(row == index, vals, 0.0), axis=0, keepdims=True)


def _pair_cols(c0, c1):
    lane = lax.broadcasted_iota(jnp.int32, (c0.shape[0], 2), 1)
    return jnp.where(lane == 0, c0, c1)


def _split3(x):
    hi = x.astype(BF16).astype(F32)
    rest = x - hi
    mid = rest.astype(BF16).astype(F32)
    return hi, mid, (rest - mid).astype(BF16).astype(F32)


def _fox_operand(vals, e, cum, is_query):
    lane = lax.broadcasted_iota(jnp.int32, (1, LANES), 1)
    base = (1 - e) * FHD
    parts = _split3(cum)
    own = jnp.where(_head_mask(e), vals * FOX_SCALE if is_query else vals, 0.0)
    cum_at, ones_at = (base, base + 3) if is_query else (base + 3, base)
    sign = 1.0 if is_query else -1.0
    out = own + jnp.where((lane >= ones_at) & (lane < ones_at + 3), 1.0, 0.0)
    for i, part in enumerate(parts):
        out = out + jnp.where(lane == cum_at + i, sign * part, 0.0)
    return out.astype(BF16)


def _causal_block():
    return lax.broadcasted_iota(jnp.int32, (BQ, BQ), 1) <= lax.broadcasted_iota(jnp.int32, (BQ, BQ), 0)


def _head_rms(o, masks):
    o2 = o * o
    r = [lax.rsqrt(jnp.sum(jnp.where(mk, o2, 0.0), axis=1, keepdims=True) * (1.0 / FHD) + EPS) for mk in masks]
    return jnp.where(masks[0], r[0], r[1])


def _hosted(exchange):
    if exchange is None:
        return [], [], [], [], []
    return (exchange.inputs, [HBM] * len(exchange.inputs), [HBM] * len(exchange.out_shape), exchange.out_shape,
            exchange.sem_shapes())


def _fox_fwd(proj, gates, w2, exchange=None):
    ex_in, ex_in_specs, ex_out_specs, ex_out_shape, ex_scratch = _hosted(exchange)

    n_in = 3 * FOX_PAIRS + 2
    heads = [(pp, e) for pp in range(FOX_PAIRS) for e in range(2)]

    def body(*refs):
        qkv_refs, g_ref, w_ref = refs[:3 * FOX_PAIRS], refs[3 * FOX_PAIRS], refs[3 * FOX_PAIRS + 1]
        mix_ref, o_ref, lse_ref = refs[n_in + len(ex_in):n_in + 3 + len(ex_in)]
        ka_ref, vb_ref = refs[n_in + 3 + len(ex_in) + len(ex_out_shape):n_in + 5 + len(ex_in) + len(ex_out_shape)]
        ex_refs = refs[n_in:n_in + len(ex_in)] + refs[n_in + 3 + len(ex_in):n_in + 3 + len(ex_in) + len(ex_out_shape)] + refs[-2:]
        grp, qi = pl.program_id(0), pl.program_id(1)

        def head_index(pp, e):
            return 2 * (FOX_PAIRS * grp + pp) + e

        if exchange is not None:
            @pl.when((grp == 0) & (qi == 0))
            def _():
                exchange.start(*exchange.split(ex_refs))

        @pl.when(qi == 0)
        def _():
            gt = g_ref[...]
            for pp in range(FOX_PAIRS):
                kv = qkv_refs[3 * pp + 1][...]
                for e in range(2):
                    ka_ref[2 * pp + e] = _fox_operand(kv, e, _lane_col(gt, head_index(pp, e)), False)
                vb_ref[pp] = qkv_refs[3 * pp + 2][...].astype(BF16)

        masks = [_head_mask(0), _head_mask(1)]
        gt = g_ref[pl.ds(pl.multiple_of(qi * BQ, BQ), BQ), :]
        qs = [_fox_operand(qkv_refs[3 * pp][...], e, _lane_col(gt, head_index(pp, e)), True) for pp, e in heads]
        n = range(len(heads))

        def block(kj, carry, diagonal):
            rows = pl.ds(pl.multiple_of(kj * BQ, BQ), BQ)
            s = [_dot(qs[i], ka_ref[i, rows, :], 1, 1) for i in n]
            if diagonal:
                s = [jnp.where(_causal_block(), s[i], -jnp.inf) for i in n]
            m_new = [jnp.maximum(carry[i][0], jnp.max(s[i], axis=-1, keepdims=True)) for i in n]
            p = [jnp.exp(s[i] - m_new[i]) for i in n]
            alpha = [jnp.exp(carry[i][0] - m_new[i]) for i in n]
            l_new = [alpha[i] * carry[i][1] + jnp.sum(p[i], axis=-1, keepdims=True) for i in n]
            pv = [_dot(p[i], vb_ref[heads[i][0], rows, :]) for i in n]
            return tuple((m_new[i], l_new[i], alpha[i] * carry[i][2] + pv[i]) for i in n)

        one = (jnp.full((BQ, 1), -jnp.inf, F32), jnp.zeros((BQ, 1), F32), jnp.zeros((BQ, LANES), F32))
        below = lax.fori_loop(0, qi, lambda kj, carry: block(kj, carry, False), (one,) * len(heads))
        done = block(qi, below, True)
        for pp in range(FOX_PAIRS):
            (m0, l0, a0), (m1, l1, a1) = done[2 * pp], done[2 * pp + 1]
            o = jnp.where(masks[0], a0 / l0, a1 / l1)
            cols = slice(pp * LANES, (pp + 1) * LANES)
            o_ref[:, cols] = o
            mix_ref[:, cols] = (o * _head_rms(o, masks) * w_ref[...]).astype(BF16)
            lse_ref[pp] = _pair_cols(m0 + jnp.log(l0), m1 + jnp.log(l1))

        if exchange is not None:
            @pl.when((grp == NPAIR // FOX_PAIRS - 1) & (qi == NQ - 1))
            def _():
                exchange.finish(*exchange.split(ex_refs))

    qkv_specs = []
    for pp in range(FOX_PAIRS):
        qkv_specs.append(pl.BlockSpec((BQ, LANES), lambda g, i, pp=pp: (i, 3 * (FOX_PAIRS * g + pp))))
        qkv_specs.append(pl.BlockSpec((S, LANES), lambda g, i, pp=pp: (0, 3 * (FOX_PAIRS * g + pp) + 1)))
        qkv_specs.append(pl.BlockSpec((S, LANES), lambda g, i, pp=pp: (0, 3 * (FOX_PAIRS * g + pp) + 2)))
    blk = pl.BlockSpec((BQ, FOX_PAIRS * LANES), lambda g, i: (i, g))
    res = pl.pallas_call(
        body, name="fox_fwd", grid=(NPAIR // FOX_PAIRS, NQ),
        in_specs=qkv_specs + [pl.BlockSpec((S, LANES), lambda g, i: (0, 0)), pl.BlockSpec((1, LANES), lambda g, i: (0, 0))]
        + ex_in_specs,
        out_specs=[blk, blk, pl.BlockSpec((FOX_PAIRS, BQ, 2), lambda g, i: (g, i, 0))] + ex_out_specs,
        out_shape=[jax.ShapeDtypeStruct((S, D), BF16), jax.ShapeDtypeStruct((S, DFOX), F32),
                   jax.ShapeDtypeStruct((NPAIR, S, 2), F32)] + ex_out_shape,
        scratch_shapes=[pltpu.VMEM((2 * FOX_PAIRS, S, LANES), BF16), pltpu.VMEM((FOX_PAIRS, S, LANES), BF16)] + ex_scratch,
        compiler_params=_cparams(),
    )(*([proj] * (3 * FOX_PAIRS)), gates, w2, *ex_in)
    return res[0], res[1], res[2], res[3:]


def _fox_norm_bwd(o, dmix, w2):
    def body(o_ref, g_ref, w_ref, do_ref, dl_ref, dw_ref):
        hp, qi = pl.program_id(0), pl.program_id(1)
        masks = [_head_mask(0), _head_mask(1)]
        ov = o_ref[...]
        g = g_ref[...]
        r = _head_rms(ov, masks)
        gw = g * w_ref[...]
        gwo = gw * ov
        mean = [jnp.sum(jnp.where(mk, gwo, 0.0), axis=1, keepdims=True) * (1.0 / FHD) for mk in masks]
        do = r * gw - ov * (r * r * r) * jnp.where(masks[0], mean[0], mean[1])
        do_ref[...] = do.astype(BF16)
        doo = do * ov
        dl_ref[...] = _pair_cols(*[jnp.sum(jnp.where(mk, doo, 0.0), axis=1, keepdims=True) for mk in masks])

        @pl.when((hp == 0) & (qi == 0))
        def _():
            dw_ref[...] = jnp.zeros_like(dw_ref)

        dw_ref[...] += jnp.sum(g * ov * r, axis=0, keepdims=True)

        @pl.when((hp == NPAIR - 1) & (qi == NQ - 1))
        def _():
            dw = dw_ref[...]
            dw_ref[...] = dw + pltpu.roll(dw, FHD, 1)

    blk = pl.BlockSpec((BQ, LANES), lambda hp, i: (i, hp))
    vec = pl.BlockSpec((1, LANES), lambda hp, i: (0, 0))
    return pl.pallas_call(
        body, name="fox_norm_bwd", grid=(NPAIR, NQ), in_specs=[blk, blk, vec],
        out_specs=[blk, pl.BlockSpec((None, BQ, 2), lambda hp, i: (hp, i, 0)), vec],
        out_shape=[jax.ShapeDtypeStruct((S, DFOX), BF16), jax.ShapeDtypeStruct((NPAIR, S, 2), F32),
                   jax.ShapeDtypeStruct((1, LANES), F32)],
        compiler_params=_cparams(),
    )(o, dmix, w2)


def _fox_bwd(proj, do, gates, lse, delta, exchange=None):
    ex_in, ex_in_specs, ex_out_specs, ex_out_shape, ex_scratch = _hosted(exchange)

    pg = FOX_PAIRS_BWD
    n_in = 3 * pg + 4
    heads = [(pp, e) for pp in range(pg) for e in range(2)]

    def body(*refs):
        qkv_refs = refs[:3 * pg]
        do_ref, g_ref, lse_ref, dl_ref = refs[3 * pg:n_in]
        dproj_ref, dc_ref = refs[n_in + len(ex_in):n_in + 2 + len(ex_in)]
        qa_ref, dq_ref = refs[n_in + 2 + len(ex_in) + len(ex_out_shape):n_in + 4 + len(ex_in) + len(ex_out_shape)]
        ex_refs = refs[n_in:n_in + len(ex_in)] + refs[n_in + 2 + len(ex_in):n_in + 2 + len(ex_in) + len(ex_out_shape)] + refs[-2:]
        grp, kj = pl.program_id(0), pl.program_id(1)

        def head_index(pp, e):
            return 2 * (pg * grp + pp) + e

        if exchange is not None:
            @pl.when((grp == 0) & (kj == 0))
            def _():
                exchange.start(*exchange.split(ex_refs))

        @pl.when(kj == 0)
        def _():
            gt = g_ref[...]
            for pp in range(pg):
                qv = qkv_refs[3 * pp][...]
                for e in range(2):
                    qa_ref[2 * pp + e] = _fox_operand(qv, e, _lane_col(gt, head_index(pp, e)), True)
            dq_ref[...] = jnp.zeros_like(dq_ref)

        @pl.when((grp == 0) & (kj == 0))
        def _():
            dc_ref[...] = jnp.zeros_like(dc_ref)

        masks = [_head_mask(0), _head_mask(1)]
        krows = pl.ds(pl.multiple_of(kj * BQ, BQ), BQ)
        gk = g_ref[krows, :]
        kas = [_fox_operand(qkv_refs[3 * pp + 1][...], e, _lane_col(gk, head_index(pp, e)), False) for pp, e in heads]
        vbs = [qkv_refs[3 * pp + 2][...].astype(BF16) for pp in range(pg)]
        lane = lax.broadcasted_iota(jnp.int32, (BQ, LANES), 1)
        n = range(len(heads))

        def block(qi, carry, diagonal):
            dks, dvs, css = carry
            rows = pl.ds(pl.multiple_of(qi * BQ, BQ), BQ)
            qa = [qa_ref[i, rows, :] for i in n]
            s = [_dot(qa[i], kas[i], 1, 1) for i in n]
            if diagonal:
                s = [jnp.where(_causal_block(), s[i], -jnp.inf) for i in n]
            dov = [do_ref[rows, pp * LANES:(pp + 1) * LANES] for pp in range(pg)]
            doe = [jnp.where(masks[e], dov[pp], jnp.zeros_like(dov[pp])) for pp, e in heads]
            lse2 = [lse_ref[pp, rows, :] for pp in range(pg)]
            dl2 = [dl_ref[pp, rows, :] for pp in range(pg)]
            p = [jnp.exp(s[i] - _lane_col(lse2[heads[i][0]], heads[i][1])) for i in n]
            dp = [_dot(doe[i], vbs[heads[i][0]], 1, 1) for i in n]
            ds = [p[i] * (dp[i] - _lane_col(dl2[heads[i][0]], heads[i][1])) for i in n]
            dv_part = [_dot(p[i], doe[i], 0, 0) for i in n]
            dk_part = [_dot(ds[i], jnp.where(masks[heads[i][1]], qa[i], jnp.zeros_like(qa[i])), 0, 0) for i in n]
            dq_part = [jnp.where(masks[heads[i][1]], _dot(ds[i], kas[i]), 0.0) for i in n]
            css = tuple(css[i] + jnp.sum(ds[i], axis=0, keepdims=True) for i in n)
            dc = jnp.zeros((BQ, LANES), F32)
            for i in n:
                dc = dc + jnp.where(lane == head_index(*heads[i]), jnp.sum(ds[i], axis=1, keepdims=True), 0.0)
            for pp in range(pg):
                dq_ref[pp, rows, :] += (dq_part[2 * pp] + dq_part[2 * pp + 1]) * FOX_SCALE
            dc_ref[rows, :] += dc
            dks = tuple(dks[pp] + dk_part[2 * pp] + dk_part[2 * pp + 1] for pp in range(pg))
            dvs = tuple(dvs[pp] + dv_part[2 * pp] + dv_part[2 * pp + 1] for pp in range(pg))
            return dks, dvs, css

        zero = jnp.zeros((BQ, LANES), F32)
        first = block(kj, ((zero,) * pg, (zero,) * pg, (jnp.zeros((1, BQ), F32),) * len(heads)), True)
        dks, dvs, css = lax.fori_loop(kj + 1, NQ, lambda qi, carry: block(qi, carry, False), first)
        r = lax.broadcasted_iota(jnp.int32, (BQ, BQ), 0)
        c = lax.broadcasted_iota(jnp.int32, (BQ, BQ), 1)
        dcol = jnp.zeros((BQ, LANES), F32)
        for i in n:
            col = jnp.sum(jnp.where(r == c, css[i], 0.0), axis=1, keepdims=True)
            dcol = dcol + jnp.where(lane == head_index(*heads[i]), col, 0.0)
        dc_ref[krows, :] -= dcol
        for pp in range(pg):
            base = 3 * pp * LANES
            dproj_ref[krows, base + LANES:base + 2 * LANES] = dks[pp].astype(BF16)
            dproj_ref[krows, base + 2 * LANES:base + 3 * LANES] = dvs[pp].astype(BF16)

        @pl.when(kj == NQ - 1)
        def _():
            for pp in range(pg):
                dproj_ref[:, 3 * pp * LANES:(3 * pp + 1) * LANES] = dq_ref[pp].astype(BF16)

        if exchange is not None:
            @pl.when((grp == NPAIR // pg - 1) & (kj == NQ - 1))
            def _():
                exchange.finish(*exchange.split(ex_refs))

    qkv_specs = []
    for pp in range(pg):
        qkv_specs.append(pl.BlockSpec((S, LANES), lambda g, j, pp=pp: (0, 3 * (pg * g + pp))))
        qkv_specs.append(pl.BlockSpec((BQ, LANES), lambda g, j, pp=pp: (j, 3 * (pg * g + pp) + 1)))
        qkv_specs.append(pl.BlockSpec((BQ, LANES), lambda g, j, pp=pp: (j, 3 * (pg * g + pp) + 2)))
    pair = pl.BlockSpec((pg, S, 2), lambda g, j: (g, 0, 0))
    res = pl.pallas_call(
        body, name="fox_bwd", grid=(NPAIR // pg, NQ),
        in_specs=qkv_specs + [pl.BlockSpec((S, pg * LANES), lambda g, j: (0, g)), pl.BlockSpec((S, LANES), lambda g, j: (0, 0)),
                              pair, pair] + ex_in_specs,
        out_specs=[pl.BlockSpec((S, 3 * pg * LANES), lambda g, j: (0, g)), pl.BlockSpec((S, LANES), lambda g, j: (0, 0))]
        + ex_out_specs,
        out_shape=[jax.ShapeDtypeStruct((S, DPROJ_PAD), BF16), jax.ShapeDtypeStruct((S, LANES), F32)] + ex_out_shape,
        scratch_shapes=[pltpu.VMEM((2 * pg, S, LANES), BF16), pltpu.VMEM((pg, S, LANES), F32)] + ex_scratch,
        compiler_params=_cparams(),
    )(*([proj] * (3 * pg)), do, gates, lse, delta, *ex_in)
    return res[0], res[1], res[2:]


NQKV = 3 * NGH
GDN_QSCALE = GHD ** -0.5


def _shift_down(x, s):
    if s == 0:
        return x
    row = lax.broadcasted_iota(jnp.int32, x.shape, 0)
    return jnp.where(row >= s, pltpu.roll(x, s, 0), 0.0)


def _shift_up(x, s):
    if s == 0:
        return x
    n = x.shape[0]
    row = lax.broadcasted_iota(jnp.int32, x.shape, 0)
    return jnp.where(row < n - s, pltpu.roll(x, n - s, 0), 0.0)


def _conv_pre(xv, wv):
    pre = xv * wv[CONV_K - 1:CONV_K, :]
    for j in range(CONV_K - 1):
        pre = pre + _shift_down(xv, CONV_K - 1 - j) * wv[j:j + 1, :]
    return pre


def _l2_factors(b):
    return b < 2 * NGH, jnp.where(b < NGH, GDN_QSCALE, 1.0)


def _gdn_pre(proj, conv_w):
    def body(x_ref, w_ref, o_ref):
        b = pl.program_id(0)
        c = _silu(_conv_pre(x_ref[...], w_ref[...]))
        normed, scale = _l2_factors(b)
        rs = lax.rsqrt(jnp.sum(c * c, axis=-1, keepdims=True) + EPS)
        o_ref[...] = c * jnp.where(normed, rs, 1.0) * scale

    return pl.pallas_call(
        body, name="gdn_pre", grid=(NQKV,),
        in_specs=[pl.BlockSpec((S, GHD), lambda b: (0, BLK_GDN + b)), pl.BlockSpec((CONV_K, GHD), lambda b: (0, b))],
        out_specs=pl.BlockSpec((S, GHD), lambda b: (0, b)),
        out_shape=jax.ShapeDtypeStruct((S, NQKV * GHD), F32), compiler_params=_cparams(),
    )(proj, conv_w)


def _gdn_pre_bwd(proj, conv_w, dqkv, dproj):
    def body(x_ref, w_ref, dy_ref, dproj_in, dx_ref, dw_ref):
        del dproj_in
        b = pl.program_id(0)
        xv = x_ref[...]
        wv = w_ref[...]
        pre = _conv_pre(xv, wv)
        sig = _sigmoid(pre)
        c = pre * sig
        normed, scale = _l2_factors(b)
        g = dy_ref[...] * scale
        rs = lax.rsqrt(jnp.sum(c * c, axis=-1, keepdims=True) + EPS)
        dc_n = rs * g - c * (rs * rs * rs) * jnp.sum(g * c, axis=-1, keepdims=True)
        dc = jnp.where(normed, dc_n, g)
        dpre = dc * sig * (1.0 + pre * (1.0 - sig))
        dx = dpre * wv[CONV_K - 1:CONV_K, :]
        for j in range(CONV_K - 1):
            dx = dx + _shift_up(dpre, CONV_K - 1 - j) * wv[j:j + 1, :]
        dx_ref[...] = dx.astype(BF16)
        for j in range(CONV_K):
            dw_ref[j:j + 1, :] = jnp.sum(dpre * _shift_down(xv, CONV_K - 1 - j), axis=0, keepdims=True)

    return pl.pallas_call(
        body, name="gdn_pre_bwd", grid=(NQKV,),
        in_specs=[pl.BlockSpec((S, GHD), lambda b: (0, BLK_GDN + b)), pl.BlockSpec((CONV_K, GHD), lambda b: (0, b)),
                  pl.BlockSpec((None, S, GHD), lambda b: (b // NGH, 0, b % NGH)), pl.BlockSpec(memory_space=pl.ANY)],
        out_specs=[pl.BlockSpec((S, GHD), lambda b: (0, BLK_GDN + b)), pl.BlockSpec((CONV_K, GHD), lambda b: (0, b))],
        out_shape=[jax.ShapeDtypeStruct((S, DPROJ_PAD), BF16), jax.ShapeDtypeStruct((CONV_K, NQKV * GHD), F32)],
        input_output_aliases={3: 0}, compiler_params=_cparams(),
    )(proj, conv_w, dqkv, dproj)


CB = 4
NCB = NCH // CB


def _chunk_prep(qs, ks, vs, gcols, bcols, t_saved=None):
    n = range(len(qs))
    r = lax.broadcasted_iota(jnp.int32, (CHUNK, CHUNK), 0)
    c = lax.broadcasted_iota(jnp.int32, (CHUNK, CHUNK), 1)
    incl = c <= r
    eye = (r == c).astype(F32)
    grow = [jnp.sum(gcols[i] * eye, axis=0, keepdims=True) for i in n]
    gc_col = [jnp.sum(jnp.where(incl, grow[i], 0.0), axis=1, keepdims=True) for i in n]
    gc_row = [jnp.sum(jnp.where(r <= c, gcols[i], 0.0), axis=0, keepdims=True) for i in n]
    decay = [jnp.exp(jnp.where(incl, gc_col[i] - gc_row[i], -jnp.inf)) for i in n]
    kb = [ks[i] * bcols[i] for i in n]
    vb = [vs[i] * bcols[i] for i in n]
    kk = [_mm_nt(kb[i], ks[i]) for i in n]
    m = [jnp.where(c < r, kk[i] * decay[i], 0.0) for i in n]
    if t_saved is None:
        t_inv = [eye - m[i] for i in n]
        p = [_hdot(m[i], m[i]) for i in n]
        for step in range(5):
            t_inv = [t_inv[i] + _hdot(t_inv[i], p[i]) for i in n]
            if step < 4:
                p = [_hdot(p[i], p[i]) for i in n]
    else:
        t_inv = [_saved_inverse(m[i], t_saved[i]) for i in n]
    egc = [jnp.exp(gc_col[i]) for i in n]
    u = [_mm_nn(t_inv[i], vb[i]) for i in n]
    w = [_mm_nn(t_inv[i], kb[i] * egc[i]) for i in n]
    qk = [_mm_nt(qs[i], ks[i]) for i in n]
    gc_last = [gc_col[i][CHUNK - 1:CHUNK, :] for i in n]
    return [(u[i], w[i], qk[i] * decay[i], qs[i] * egc[i], ks[i] * jnp.exp(gc_last[i] - gc_col[i]), jnp.exp(gc_last[i]),
             t_inv[i]) for i in n]


def _prep_specs():
    rows = CB * CHUNK
    qs = pl.BlockSpec((rows, GHD), lambda i, h: (i, h))
    ks = pl.BlockSpec((rows, GHD), lambda i, h: (i, NGH + h))
    vs = pl.BlockSpec((rows, GHD), lambda i, h: (i, 2 * NGH + h))
    gs = pl.BlockSpec((rows, LANES), lambda i, h: (i, 0))
    a_s = pl.BlockSpec((None, rows, CHUNK), lambda i, h: (h, i, 0))
    gl_s = pl.BlockSpec((None, CB, 1, LANES), lambda i, h: (h, i, 0, 0))
    return qs, ks, vs, gs, a_s, gl_s


def _gdn_prep(qkv, gates):
    def body(q_ref, k_ref, v_ref, g_ref, u_ref, w_ref, qd_ref, kd_ref, a_ref, gl_ref, t_ref):
        h = pl.program_id(1)
        chunks = [pl.ds(cidx * CHUNK, CHUNK) for cidx in range(CB)]
        gts = [g_ref[rows, :] for rows in chunks]
        outs = _chunk_prep([q_ref[rows, :] for rows in chunks], [k_ref[rows, :] for rows in chunks],
                           [v_ref[rows, :] for rows in chunks], [_lane_col(gt, LANE_G + h) for gt in gts],
                           [_lane_col(gt, LANE_BETA + h) for gt in gts])
        for cidx, rows in enumerate(chunks):
            u, w, a, qd, kd, gl, t_inv = outs[cidx]
            u_ref[rows, :] = u
            w_ref[rows, :] = w
            qd_ref[rows, :] = qd
            kd_ref[rows, :] = kd
            a_ref[rows, :] = a
            t_ref[rows, :] = t_inv
            gl_ref[cidx] = jnp.broadcast_to(gl, (1, LANES))

    qs, ks, vs, gs, a_s, gl_s = _prep_specs()
    tok = jax.ShapeDtypeStruct((S, DGDN), F32)
    sq = jax.ShapeDtypeStruct((NGH, S, CHUNK), F32)
    return pl.pallas_call(
        body, name="gdn_prep", grid=(NCB, NGH), in_specs=[qs, ks, vs, gs], out_specs=[qs, qs, qs, qs, a_s, gl_s, a_s],
        out_shape=[tok, tok, tok, tok, sq, jax.ShapeDtypeStruct((NGH, NCH, 1, LANES), F32), sq],
        compiler_params=_cparams(),
    )(qkv, qkv, qkv, gates)


def _gdn_prep_bwd(qkv, gates, t_inv, du, dw, dqd, dkd, da, dgl):
    def body(q_ref, k_ref, v_ref, g_ref, t_ref, du_ref, dw_ref, dqd_ref, dkd_ref, da_ref, dgl_ref, dqkv_ref, dg_ref):
        h = pl.program_id(1)

        @pl.when(h == 0)
        def _():
            dg_ref[...] = jnp.zeros_like(dg_ref)

        lane = lax.broadcasted_iota(jnp.int32, (CHUNK, LANES), 1)
        chunks = [pl.ds(cidx * CHUNK, CHUNK) for cidx in range(CB)]
        gts = [g_ref[rows, :] for rows in chunks]
        t_saved = [t_ref[rows, :] for rows in chunks]
        _, vjp = jax.vjp(lambda *args: [o[:6] for o in _chunk_prep(*args, t_saved=t_saved)],
                         [q_ref[rows, :] for rows in chunks], [k_ref[rows, :] for rows in chunks],
                         [v_ref[rows, :] for rows in chunks], [_lane_col(gt, LANE_G + h) for gt in gts],
                         [_lane_col(gt, LANE_BETA + h) for gt in gts])
        dqs, dks, dvs, dgcs, dbcs = vjp([(du_ref[rows, :], dw_ref[rows, :], da_ref[rows, :], dqd_ref[rows, :],
                                          dkd_ref[rows, :], dgl_ref[cidx][:, 0:1]) for cidx, rows in enumerate(chunks)])
        for cidx, rows in enumerate(chunks):
            dq, dk, dv, dgc, dbc = dqs[cidx], dks[cidx], dvs[cidx], dgcs[cidx], dbcs[cidx]
            dqkv_ref[0, rows, :] = dq
            dqkv_ref[1, rows, :] = dk
            dqkv_ref[2, rows, :] = dv
            dg_ref[rows, :] += jnp.where(lane == LANE_G + h, dgc, 0.0) + jnp.where(lane == LANE_BETA + h, dbc, 0.0)

    qs, ks, vs, gs, a_s, gl_s = _prep_specs()
    return pl.pallas_call(
        body, name="gdn_prep_bwd", grid=(NCB, NGH), in_specs=[qs, ks, vs, gs, a_s, qs, qs, qs, qs, a_s, gl_s],
        out_specs=[pl.BlockSpec((3, CB * CHUNK, GHD), lambda i, h: (0, i, h)), gs],
        out_shape=[jax.ShapeDtypeStruct((3, S, DGDN), F32), jax.ShapeDtypeStruct((S, LANES), F32)],
        compiler_params=_cparams(),
    )(qkv, qkv, qkv, gates, t_inv, du, dw, dqd, dkd, da, dgl)


def _scan_specs():
    hs = pl.BlockSpec((S, GHD), lambda h: (0, h))
    a_s = pl.BlockSpec((None, S, CHUNK), lambda h: (h, 0, 0))
    gl_s = pl.BlockSpec((None, NCH, 1, LANES), lambda h: (h, 0, 0, 0))
    st_s = pl.BlockSpec((None, NCH, GHD, GHD), lambda h: (h, 0, 0, 0))
    gz_s = pl.BlockSpec((S, GHD), lambda h: (0, BLK_GZ + h))
    mix_s = pl.BlockSpec((S, GHD), lambda h: (0, NPAIR + h))
    return hs, a_s, gl_s, st_s, gz_s, mix_s


def _gdn_scan(u, w, qd, kd, a, gl, proj, w_norm, mix):
    def body(u_ref, w_ref, qd_ref, kd_ref, a_ref, gl_ref, z_ref, wn_ref, mix_in, mix_ref, o_ref, st_ref):
        del mix_in

        def step(ci, state):
            rows = pl.ds(pl.multiple_of(ci * CHUNK, CHUNK), CHUNK)
            st_ref[ci] = state
            vn = u_ref[rows, :] - _dot(w_ref[rows, :], state)
            o_ref[rows, :] = _dot(qd_ref[rows, :], state) + _dot(a_ref[rows, :], vn)
            return state * gl_ref[ci] + _dot(kd_ref[rows, :], vn, 0, 0)

        lax.fori_loop(0, NCH, step, jnp.zeros((GHD, GHD), F32))
        ov = o_ref[...]
        mix_ref[...] = (ov * _rms_scale(ov) * wn_ref[...] * _silu(z_ref[...])).astype(BF16)

    hs, a_s, gl_s, st_s, gz_s, mix_s = _scan_specs()
    return pl.pallas_call(
        body, name="gdn_scan", grid=(NGH,),
        in_specs=[hs, hs, hs, hs, a_s, gl_s, gz_s, pl.BlockSpec((1, GHD), lambda h: (0, 0)), pl.BlockSpec(memory_space=pl.ANY)],
        out_specs=[mix_s, hs, st_s],
        out_shape=[jax.ShapeDtypeStruct((S, D), BF16), jax.ShapeDtypeStruct((S, DGDN), F32),
                   jax.ShapeDtypeStruct((NGH, NCH, GHD, GHD), F32)],
        input_output_aliases={8: 0}, compiler_params=_cparams(),
    )(u, w, qd, kd, a, gl, proj, w_norm, mix)


def _gdn_scan_bwd(dmix, o, proj, w_norm, u, w, qd, kd, a, gl, states, dproj):
    def body(dy_ref, o_ref, z_ref, wn_ref, u_ref, w_ref, qd_ref, kd_ref, a_ref, gl_ref, st_ref, dproj_in,
             dz_ref, du_ref, dw_ref, dqd_ref, dkd_ref, da_ref, dgl_ref, dwn_ref, do_ref):
        del dproj_in
        h = pl.program_id(0)
        ov = o_ref[...]
        zv = z_ref[...]
        wn = wn_ref[...]
        g = dy_ref[...]
        sig = _sigmoid(zv)
        dz_ref[...] = (g * (ov * _rms_scale(ov) * wn) * sig * (1.0 + zv * (1.0 - sig))).astype(BF16)
        do, dwt = _rms_bwd(ov, wn, g * zv * sig)
        do_ref[...] = do

        @pl.when(h == 0)
        def _():
            dwn_ref[...] = jnp.zeros_like(dwn_ref)

        dwn_ref[...] += jnp.sum(dwt, axis=0, keepdims=True)

        def step(t, dstate):
            ci = NCH - 1 - t
            rows = pl.ds(pl.multiple_of(ci * CHUNK, CHUNK), CHUNK)
            state = st_ref[ci]
            dov = do_ref[rows, :]
            wv = w_ref[rows, :]
            kdv = kd_ref[rows, :]
            vn = u_ref[rows, :] - _dot(wv, state)
            dvn = _dot(a_ref[rows, :], dov, 0, 0) + _dot(kdv, dstate)
            da_ref[rows, :] = _dot(dov, vn, 1, 1)
            dqd_ref[rows, :] = _dot(dov, state, 1, 1)
            dkd_ref[rows, :] = _dot(vn, dstate, 1, 1)
            dgl = jnp.sum(jnp.sum(dstate * state, axis=1, keepdims=True), axis=0, keepdims=True)
            dgl_ref[ci] = jnp.broadcast_to(dgl, (1, LANES))
            du_ref[rows, :] = dvn
            dw_ref[rows, :] = -_dot(dvn, state, 1, 1)
            return dstate * gl_ref[ci] + _dot(qd_ref[rows, :], dov, 0, 0) - _dot(wv, dvn, 0, 0)

        lax.fori_loop(0, NCH, step, jnp.zeros((GHD, GHD), F32))

    hs, a_s, gl_s, st_s, gz_s, mix_s = _scan_specs()
    vec = pl.BlockSpec((1, GHD), lambda h: (0, 0))
    tok = jax.ShapeDtypeStruct((S, DGDN), F32)
    return pl.pallas_call(
        body, name="gdn_scan_bwd", grid=(NGH,),
        in_specs=[mix_s, hs, gz_s, vec, hs, hs, hs, hs, a_s, gl_s, st_s, pl.BlockSpec(memory_space=pl.ANY)],
        out_specs=[gz_s, hs, hs, hs, hs, a_s, gl_s, vec],
        out_shape=[jax.ShapeDtypeStruct((S, DPROJ_PAD), BF16), tok, tok, tok, tok,
                   jax.ShapeDtypeStruct((NGH, S, CHUNK), F32), jax.ShapeDtypeStruct((NGH, NCH, 1, LANES), F32),
                   jax.ShapeDtypeStruct((1, GHD), F32)],
        input_output_aliases={11: 0}, scratch_shapes=[pltpu.VMEM((S, GHD), F32)], compiler_params=_cparams(),
    )(dmix, o, proj, w_norm, u, w, qd, kd, a, gl, states, dproj)


def _place():
    return lax.axis_index("x"), lax.axis_index("y"), lax.axis_index("c")


def _other_chips(x, y):
    return [(1 - x, y), (x, 1 - y), (1 - x, 1 - y)]


HBM = pl.BlockSpec(memory_space=pltpu.HBM)
VMEM = pl.BlockSpec(memory_space=pltpu.VMEM)


def _half_rows(ref_or_rows, half):
    rows = ref_or_rows // 2
    return pl.ds(pl.multiple_of(half * rows, rows), rows)


class _Exchange:
    def __init__(self, inputs, out_shape, n_sems, start, finish):
        self.inputs, self.out_shape, self.n_sems, self.start, self.finish = inputs, out_shape, n_sems, start, finish

    def sem_shapes(self):
        return [pltpu.SemaphoreType.DMA((self.n_sems,)), pltpu.SemaphoreType.DMA((self.n_sems,))]

    def split(self, refs):
        n_in, n_out = len(self.inputs), len(self.out_shape)
        return refs[:n_in], refs[n_in:n_in + n_out], refs[n_in + n_out], refs[n_in + n_out + 1]


def _run_exchange(ex, name):
    def body(*refs):
        parts = ex.split(refs)
        ex.start(*parts)
        ex.finish(*parts)

    return pl.pallas_call(
        body, name=name, in_specs=[HBM] * len(ex.inputs), out_specs=[HBM] * len(ex.out_shape), out_shape=ex.out_shape,
        scratch_shapes=ex.sem_shapes(), compiler_params=_cparams(),
    )(*ex.inputs)


def _allgather_exchange(shards, whole=()):
    n, nw = len(shards), len(whole)

    def plan(src, outs, send_sems, recv_sems):
        x, y, c = _place()
        chips = _other_chips(x, y)
        chip_ids = [2 * ch[0] + ch[1] for ch in chips]

        def copy(a, k, chip_index, half, to, from_src):
            rows = _half_rows(src[a].shape[0], half)
            dst = outs[a].at[chip_index, rows]
            return pltpu.make_async_remote_copy(
                src_ref=src[a].at[rows] if from_src else dst, dst_ref=dst, send_sem=send_sems.at[6 * a + k],
                recv_sem=recv_sems.at[6 * a + k], device_id=to, device_id_type=MESH)

        def whole_copy(b, k, chip_index, to):
            return pltpu.make_async_remote_copy(
                src_ref=src[n + b], dst_ref=outs[n + b].at[chip_index], send_sem=send_sems.at[6 * n + 3 * b + k],
                recv_sem=recv_sems.at[6 * n + 3 * b + k], device_id=to, device_id_type=MESH)

        me, sibling = (x, y, c), (x, y, 1 - c)
        first = [copy(a, j, 2 * x + y, c, (*chips[j], c), True) for a in range(n) for j in range(3)]
        first += [whole_copy(b, j, 2 * x + y, (*chips[j], c)) for b in range(nw) for j in range(3)]
        landing = [copy(a, j, chip_ids[j], c, me, False) for a in range(n) for j in range(3)]
        passed = [copy(a, 3 + j, chip_ids[j], c, sibling, False) for a in range(n) for j in range(3)]
        arriving = [copy(a, 3 + j, chip_ids[j], 1 - c, me, False) for a in range(n) for j in range(3)]
        arriving += [whole_copy(b, j, chip_ids[j], me) for b in range(nw) for j in range(3)]
        return first, landing, passed, arriving

    def start(*refs):
        for cp in plan(*refs)[0]:
            cp.start()

    def finish(*refs):
        first, landing, passed, arriving = plan(*refs)
        for lands, onward in zip(landing, passed):
            lands.wait_recv()
            onward.start()
        for cp in arriving:
            cp.wait_recv()
        for cp in first + passed:
            cp.wait_send()

    out_shape = [jax.ShapeDtypeStruct((NCHIP,) + s.shape, s.dtype) for s in list(shards) + list(whole)]
    return _Exchange(list(shards) + list(whole), out_shape, 6 * n + 3 * nw, start, finish)


def _with_own(gathered, own):
    x, y, _ = _place()
    return lax.dynamic_update_index_in_dim(gathered, own, 2 * x + y, axis=0)


def _simple_exchange(inputs, out_shape, copies_of):
    def start(*refs):
        for cp in copies_of(*refs):
            cp.start()

    def finish(*refs):
        for cp in copies_of(*refs):
            cp.wait()

    return _Exchange(list(inputs), out_shape, len(out_shape) * 3, start, finish)


def _pair_exchange(grads):
    def copies_of(src, outs, send_sems, recv_sems):
        x, y, c = _place()
        return [pltpu.make_async_remote_copy(
            src_ref=src[a].at[:, _half_rows(src[a].shape[1], 1 - c)], dst_ref=outs[a], send_sem=send_sems.at[a],
            recv_sem=recv_sems.at[a], device_id=(x, y, 1 - c), device_id_type=MESH) for a in range(len(src))]

    return _simple_exchange(
        grads, [jax.ShapeDtypeStruct((g.shape[0], g.shape[1] // 2, g.shape[2]), g.dtype) for g in grads], copies_of)


def _pair_sum(grads, theirs, name):
    n = len(grads)

    def body(*refs):
        south = lax.axis_index("c") == 0
        for a in range(n):
            g = refs[a][...]
            half = g.shape[0] // 2
            mine = jnp.where(south, g[:half], g[half:])
            refs[2 * n + a][...] = (mine.astype(F32) + refs[n + a][...].astype(F32)).astype(BF16)

    def specs(arrs):
        return [pl.BlockSpec((None,) + g.shape[1:], lambda j: (j, 0, 0)) for g in arrs]

    return pl.pallas_call(
        body, name=name, grid=(NCHIP,), in_specs=specs(grads) + specs(theirs), out_specs=specs(theirs),
        out_shape=[jax.ShapeDtypeStruct(g.shape, BF16) for g in theirs], compiler_params=_cparams(),
    )(*grads, *theirs)


def _chip_exchange(parts):
    def copies_of(src, outs, send_sems, recv_sems):
        x, y, c = _place()
        return [pltpu.make_async_remote_copy(
            src_ref=src[a].at[2 * chip[0] + chip[1]], dst_ref=outs[a].at[k], send_sem=send_sems.at[3 * a + k],
            recv_sem=recv_sems.at[3 * a + k], device_id=(*chip, c), device_id_type=MESH)
            for a in range(len(src)) for k, chip in enumerate(_other_chips(x, y))]

    return _simple_exchange(parts, [jax.ShapeDtypeStruct((NCHIP - 1,) + p.shape[1:], p.dtype) for p in parts], copies_of)


def _chip_sum(parts, received):
    n = len(parts)
    steps = 4

    def body(*refs):
        chip = 2 * lax.axis_index("x") + lax.axis_index("y")
        for a in range(n):
            p, r = refs[a], refs[n + a]
            own = jnp.where(chip == 0, p[0], jnp.where(chip == 1, p[1], jnp.where(chip == 2, p[2], p[3])))
            refs[2 * n + a][...] = ((own.astype(F32) + r[0].astype(F32)) + r[1].astype(F32)) + r[2].astype(F32)

    def specs(arrs):
        return [pl.BlockSpec((g.shape[0], g.shape[1] // steps, g.shape[2]), lambda i: (0, i, 0)) for g in arrs]

    out_specs = [pl.BlockSpec((g.shape[1] // steps, g.shape[2]), lambda i: (i, 0)) for g in parts]
    return pl.pallas_call(
        body, name="grads_chip_sum", grid=(steps,), in_specs=specs(parts) + specs(received), out_specs=out_specs,
        out_shape=[jax.ShapeDtypeStruct(g.shape[1:], F32) for g in parts], compiler_params=_cparams(),
    )(*parts, *received)


def _pair_share(halves):
    def copies_of(src, outs, send_sems, recv_sems):
        x, y, c = _place()
        return [pltpu.make_async_remote_copy(
            src_ref=src[a], dst_ref=outs[a], send_sem=send_sems.at[a], recv_sem=recv_sems.at[a],
            device_id=(x, y, 1 - c), device_id_type=MESH) for a in range(len(src))]

    return _simple_exchange(halves, [jax.ShapeDtypeStruct(h.shape, F32) for h in halves], copies_of)


def _adamw_math(w, g, m, v):
    nm = ADAM_B1 * m + (1.0 - ADAM_B1) * g
    nv = ADAM_B2 * v + (1.0 - ADAM_B2) * jnp.square(g)
    m_hat = nm / (1.0 - ADAM_B1 ** ADAM_STEP)
    v_hat = nv / (1.0 - ADAM_B2 ** ADAM_STEP)
    return -ADAM_LR * (m_hat / (jnp.sqrt(v_hat) + ADAM_EPS) + ADAM_WD * w), nm, nv


def _adamw_big(ws, g_mine, g_theirs, ms, vs):
    n = len(ws)
    steps = 8

    def body(*refs):
        own_half = (pl.program_id(0) // (steps // 2)) == lax.axis_index("c")
        for a in range(n):
            w = refs[a][...]
            g = jnp.where(own_half, refs[n + a][...], refs[2 * n + a][...])[:, :w.shape[1]]
            d, nm, nv = _adamw_math(w, g, refs[3 * n + a][...], refs[4 * n + a][...])
            refs[5 * n + a][...] = g
            refs[6 * n + a][...] = d
            refs[7 * n + a][...] = nm
            refs[8 * n + a][...] = nv

    specs = [pl.BlockSpec((w.shape[0] // steps, w.shape[1]), lambda i: (i, 0)) for w in ws]
    half_specs = [pl.BlockSpec((g.shape[0] // (steps // 2), g.shape[1]), lambda i: (i % (steps // 2), 0)) for g in g_mine]
    shapes = [jax.ShapeDtypeStruct(w.shape, F32) for w in ws]
    res = pl.pallas_call(
        body, name="adamw_big", grid=(steps,), in_specs=specs + half_specs * 2 + specs * 2, out_specs=specs * 4,
        out_shape=shapes * 4, compiler_params=_cparams(),
    )(*ws, *g_mine, *g_theirs, *ms, *vs)
    return res[:n], res[n:2 * n], res[2 * n:3 * n], res[3 * n:]


NORM_NAMES = ("pre_mix_norm", "post_mix_norm", "pre_mlp_norm", "post_mlp_norm")
SMALL_NAMES = NORM_NAMES + ("gdn_conv_w", "fox_f_bias", "gdn_dt_bias", "gdn_a_log", "fox_out_norm", "gdn_out_norm")
CONV_COLS = 3 * DGDN // NCHIP


def _small_allreduce(d_norms, d_conv, sums, d_fox_norm, d_gdn_norm):
    def body(*refs):
        dn_ref, dconv_ref, sums_ref, dfn_ref, dgn_ref = refs[:5]
        outs = refs[5:10]
        g_norms, g_conv, g_sums, g_fn, g_gn, send_sems, recv_sems, local_sem = refs[10:]
        x, y, c = _place()
        me = 4 * x + 2 * y + c
        g_norms[me] = dn_ref[...]
        g_sums[me] = sums_ref[...]
        g_fn[me] = dfn_ref[...]
        g_gn[me] = dgn_ref[...]

        def conv_cols(chip_index):
            return dconv_ref.at[:, pl.ds(pl.multiple_of(chip_index * CONV_COLS, LANES), CONV_COLS)]

        own = pltpu.make_async_copy(conv_cols(2 * x + y), g_conv.at[me], local_sem)
        own.start()
        copies = []
        for k in range(1, NDEV):
            px, py, pc = x ^ ((k >> 2) & 1), y ^ ((k >> 1) & 1), c ^ (k & 1)
            pairs = [(dn_ref, g_norms), (conv_cols(2 * px + py), g_conv), (sums_ref, g_sums), (dfn_ref, g_fn), (dgn_ref, g_gn)]
            for a, (src, dst) in enumerate(pairs):
                cp = pltpu.make_async_remote_copy(
                    src_ref=src, dst_ref=dst.at[me], send_sem=send_sems.at[5 * (k - 1) + a],
                    recv_sem=recv_sems.at[5 * (k - 1) + a], device_id=(px, py, pc), device_id_type=MESH)
                cp.start()
                copies.append(cp)
        own.wait()
        for cp in copies:
            cp.wait()

        def total(buf):
            acc = buf[0]
            for i in range(1, NDEV):
                acc = acc + buf[i]
            return acc

        for out, buf in zip(outs, (g_norms, g_conv, g_sums, g_fn, g_gn)):
            out[...] = total(buf)

    n_sem = 5 * (NDEV - 1)
    shapes = [(4, D), (CONV_K, CONV_COLS), (8, LANES), (1, LANES), (1, LANES)]
    return pl.pallas_call(
        body, name="small_allreduce", in_specs=[VMEM] * 5, out_specs=[VMEM] * 5,
        out_shape=[jax.ShapeDtypeStruct(s, F32) for s in shapes],
        scratch_shapes=[pltpu.VMEM((NDEV,) + s, F32) for s in shapes]
        + [pltpu.SemaphoreType.DMA((n_sem,)), pltpu.SemaphoreType.DMA((n_sem,)), pltpu.SemaphoreType.DMA],
        compiler_params=_cparams(),
    )(d_norms, d_conv, sums, d_fox_norm, d_gdn_norm)


def _small_adamw(totals, ws, ms, vs):
    n = len(SMALL_NAMES)

    def body(*refs):
        t_norms, t_conv, t_sums, t_fn, t_gn = [r[...] for r in refs[:5]]
        w_refs, m_refs, v_refs = refs[5:5 + n], refs[5 + n:5 + 2 * n], refs[5 + 2 * n:5 + 3 * n]
        outs = refs[5 + 3 * n:]
        grads = [t_norms[i:i + 1, :] for i in range(4)] + [
            t_conv, t_sums[0:1, 0:NFH], t_sums[1:2, 0:NGH], t_sums[2:3, 0:NGH], t_fn[:, 0:FHD], t_gn]
        for a in range(n):
            d, nm, nv = _adamw_math(w_refs[a][...], grads[a], m_refs[a][...], v_refs[a][...])
            outs[a][...] = grads[a]
            outs[n + a][...] = d
            outs[2 * n + a][...] = nm
            outs[3 * n + a][...] = nv

    def whole(arr):
        return pl.BlockSpec(arr.shape, lambda i: (0, 0))

    res = pl.pallas_call(
        body, name="small_adamw", grid=(1,), in_specs=[whole(t) for t in totals] + [whole(w) for w in ws] * 3,
        out_specs=[whole(w) for w in ws] * 4, out_shape=[jax.ShapeDtypeStruct(w.shape, F32) for w in ws] * 4,
        compiler_params=_cparams(),
    )(*totals, *ws, *ms, *vs)
    return res[:n], res[n:2 * n], res[2 * n:3 * n], res[3 * n:]


def _to_padded_cols(w):
    pieces = [w[:, part * DFOX + hp * LANES:part * DFOX + (hp + 1) * LANES] for hp in range(NPAIR) for part in range(3)]
    pieces += [w[:, 1544:3080], w[:, 3088:3600], w[:, 1536:1544], w[:, 3080:3088],
               jnp.zeros((w.shape[0], 2 * LANES - 16), w.dtype)]
    return jnp.concatenate(pieces, axis=1)


def _from_padded_cols(w):
    c0 = BLK_SMALL * LANES
    fox = [w[:, (3 * hp + part) * LANES:(3 * hp + part + 1) * LANES] for part in range(3) for hp in range(NPAIR)]
    return jnp.concatenate(fox + [w[:, c0:c0 + 8], w[:, BLK_GDN * LANES:BLK_GZ * LANES], w[:, c0 + 8:c0 + 16],
                                  w[:, BLK_GZ * LANES:BLK_SMALL * LANES]], axis=1)


def _local_step(x, target, win_p, late_weights, reduce_mlp, pre_mix_norm, fox_f_bias, fox_out_norm, conv_w, gdn_a_log,
                gdn_dt_bias, gdn_out_norm, post_mix_norm, pre_mlp_norm, post_mlp_norm):
    bias_vec = jnp.zeros((1, LANES), F32).at[0, 0:NFH].set(fox_f_bias).at[0, LANE_G:LANE_G + NGH].set(gdn_dt_bias)
    alog_vec = jnp.zeros((1, LANES), F32).at[0, LANE_G:LANE_G + NGH].set(gdn_a_log)
    w2 = jnp.concatenate([fox_out_norm, fox_out_norm], axis=1)

    h = _pre_norm(x, pre_mix_norm)
    proj = _matmul(h, win_p, tm=2048, tn=768, tk=1024, name="mm_proj")
    gates = _gates(proj, bias_vec, alog_vec)
    mix, fox_o, lse, late = _fox_fwd(proj, gates, w2, exchange=late_weights[0])
    wout, wup3, wdown = late_weights[1](late)
    qkv = _gdn_pre(proj, conv_w)
    u, w, qd, kd, a_intra, gl, t_inv = _gdn_prep(qkv, gates)
    mix, gdn_raw, states = _gdn_scan(u, w, qd, kd, a_intra, gl, proj, gdn_out_norm, mix)
    mixed = _matmul(mix, wout, tm=2048, tk=1024, name="mm_out")
    x1, h2 = _post_mix(x, mixed, post_mix_norm, pre_mlp_norm)

    def relu2(acc):
        r = jnp.maximum(acc, 0.0)
        return acc, r * r

    up, act = _matmul(h2, wup3, b3=True, tm=1024, tn=1024, tk=1024, out_dtypes=(F32, BF16), epilogue=relu2, name="mm_up")
    y = _matmul(act, wdown, tm=2048, tk=1024, name="mm_down")
    dx2, dy, d_post_mlp, loss_row = _loss_head(x1, y, post_mlp_norm, target)

    dwdown = _matmul(act, dy, ta=True, tm=1024, tn=1024, tk=2048, out_dtypes=(BF16,), name="mm_dwdown")

    def relu2_bwd(acc, upv):
        return (acc * 2.0 * jnp.maximum(upv, 0.0),)

    dup = _matmul(dy, wdown, tb=True, tm=1024, tn=1024, tk=1024, out_dtypes=(BF16,), extra=(up,), epilogue=relu2_bwd,
                  name="mm_dact")
    dwup3 = _matmul(h2, dup, ta=True, tm=1024, tn=1024, tk=2048, out_dtypes=(BF16,), o3=True, name="mm_dwup")
    dh2 = _matmul(dup, wup3, tb=True, b3=True, tm=2048, tk=1024, name="mm_dh2")
    dx1, dmixed, d_pre_mlp, d_post_mix = _mid_bwd(dh2, x1, pre_mlp_norm, dx2, mixed, post_mix_norm)
    dwout = _matmul(mix, dmixed, ta=True, tm=1024, tn=1024, tk=2048, out_dtypes=(BF16,), name="mm_dwout")
    dmix = _matmul(dmixed, wout, tb=True, tm=2048, tk=1024, name="mm_dmix")

    dfox, delta, d_fox_norm = _fox_norm_bwd(fox_o, dmix, w2)
    dproj, dcum_fox, reduced_mlp = _fox_bwd(proj, dfox, gates, lse, delta, exchange=reduce_mlp(dwup3, dwdown))
    dproj, du, dw, dqd, dkd, da, dgl, d_gdn_norm = _gdn_scan_bwd(dmix, gdn_raw, proj, gdn_out_norm, u, w, qd, kd,
                                                                 a_intra, gl, states, dproj)
    dqkv, dgates_gdn = _gdn_prep_bwd(qkv, gates, t_inv, du, dw, dqd, dkd, da, dgl)
    dproj, d_conv = _gdn_pre_bwd(proj, conv_w, dqkv, dproj)
    dproj, sums = _gates_bwd(proj, bias_vec, alog_vec, dgates_gdn, dcum_fox, dproj)

    dwin_p = _matmul(h, dproj, ta=True, tm=1024, tn=1280, tk=2048, out_dtypes=(BF16,), name="mm_dwin")
    dh = _matmul(dproj, win_p, tb=True, tm=2048, tk=1280, name="mm_dh")
    grad_x, d_pre_mix = _pre_norm_bwd(dh, x, pre_mix_norm, dx1)

    d_norms = jnp.concatenate([d_pre_mix, d_post_mix, d_pre_mlp, d_post_mlp], axis=0)
    return (loss_row[0, 0], grad_x, (dwin_p, dwout, dwup3, dwdown), (d_norms, d_conv, sums, d_fox_norm, d_gdn_norm),
            reduced_mlp)


def kernel(x, pre_mix_norm, w_in, fox_f_bias, fox_out_norm, gdn_conv_w, gdn_a_log, gdn_dt_bias, gdn_out_norm, w_out, post_mix_norm, pre_mlp_norm, w_up, w_down, post_mlp_norm, loss_target, m_pre_mix_norm, m_w_in, m_fox_f_bias, m_fox_out_norm, m_gdn_conv_w, m_gdn_a_log, m_gdn_dt_bias, m_gdn_out_norm, m_w_out, m_post_mix_norm, m_pre_mlp_norm, m_w_up, m_w_down, m_post_mlp_norm, v_pre_mix_norm, v_w_in, v_fox_f_bias, v_fox_out_norm, v_gdn_conv_w, v_gdn_a_log, v_gdn_dt_bias, v_gdn_out_norm, v_w_out, v_post_mix_norm, v_pre_mlp_norm, v_w_up, v_w_down, v_post_mlp_norm):
    weights = dict(pre_mix_norm=pre_mix_norm, w_in=w_in, fox_f_bias=fox_f_bias, fox_out_norm=fox_out_norm, gdn_conv_w=gdn_conv_w,
                   gdn_a_log=gdn_a_log, gdn_dt_bias=gdn_dt_bias, gdn_out_norm=gdn_out_norm, w_out=w_out, post_mix_norm=post_mix_norm,
                   pre_mlp_norm=pre_mlp_norm, w_up=w_up, w_down=w_down, post_mlp_norm=post_mlp_norm)
    m_in = dict(pre_mix_norm=m_pre_mix_norm, w_in=m_w_in, fox_f_bias=m_fox_f_bias, fox_out_norm=m_fox_out_norm, gdn_conv_w=m_gdn_conv_w,
                gdn_a_log=m_gdn_a_log, gdn_dt_bias=m_gdn_dt_bias, gdn_out_norm=m_gdn_out_norm, w_out=m_w_out, post_mix_norm=m_post_mix_norm,
                pre_mlp_norm=m_pre_mlp_norm, w_up=m_w_up, w_down=m_w_down, post_mlp_norm=m_post_mlp_norm)
    v_in = dict(pre_mix_norm=v_pre_mix_norm, w_in=v_w_in, fox_f_bias=v_fox_f_bias, fox_out_norm=v_fox_out_norm, gdn_conv_w=v_gdn_conv_w,
                gdn_a_log=v_gdn_a_log, gdn_dt_bias=v_gdn_dt_bias, gdn_out_norm=v_gdn_out_norm, w_out=v_w_out, post_mix_norm=v_post_mix_norm,
                pre_mlp_norm=v_pre_mlp_norm, w_up=v_w_up, w_down=v_w_down, post_mlp_norm=v_post_mlp_norm)
    order_w = ("pre_mix_norm", "w_in", "fox_f_bias", "fox_out_norm", "gdn_conv_w", "gdn_a_log", "gdn_dt_bias", "gdn_out_norm", "w_out",
               "post_mix_norm", "pre_mlp_norm", "w_up", "w_down", "post_mlp_norm")
    big = ("w_in", "w_out", "w_up", "w_down")

    def row(v):
        return v if v.ndim == 2 else v.reshape(1, -1)

    cw = DPROJ // NCHIP
    win_shard = jnp.pad(w_in.astype(BF16), ((0, 0), (0, D - cw)))
    win_g, conv_g = _run_exchange(_allgather_exchange([win_shard], whole=[gdn_conv_w]), "weights_allgather_in")
    win_p = _to_padded_cols(_with_own(win_g, win_shard)[:, :, :cw].transpose(1, 0, 2).reshape(D, DPROJ))
    conv_full = _with_own(conv_g, gdn_conv_w).transpose(1, 0, 2).reshape(CONV_K, 3 * DGDN)
    late_shards = [weights[n].astype(BF16) for n in big[1:]]

    def resolve_late(gathered):
        wout_g, wup3, wdown_g = [_with_own(g, own) for g, own in zip(gathered, late_shards)]
        return wout_g.reshape(D, D), wup3, wdown_g.reshape(DFF, D)

    mlp_sums = []

    def reduce_mlp(dwup3, dwdown):
        blocks = [dwup3, dwdown.reshape(NCHIP, DFF // NCHIP, D)]
        theirs = _run_exchange(_pair_exchange(blocks), "grads_pair_exchange_mlp")
        mlp_sums.extend(_pair_sum(blocks, theirs, "grads_pair_sum_mlp"))
        return _chip_exchange(mlp_sums)

    loss_local, grad_x, (dwin_p, dwout, _, _), small, mlp_received = _local_step(
        x[0], loss_target[0], win_p, (_allgather_exchange(late_shards), resolve_late), reduce_mlp, row(pre_mix_norm),
        fox_f_bias, row(fox_out_norm), conv_full, gdn_a_log, gdn_dt_bias, row(gdn_out_norm), row(post_mix_norm),
        row(pre_mlp_norm), row(post_mlp_norm))
    loss = lax.psum(loss_local, ("x", "y", "c"))

    dwin3 = jnp.pad(_from_padded_cols(dwin_p).reshape(D, NCHIP, cw).transpose(1, 0, 2), ((0, 0), (0, 0), (0, D - cw)))
    blocks = [dwin3, dwout.reshape(NCHIP, D // NCHIP, D)]
    pair_sums = _pair_sum(blocks, _run_exchange(_pair_exchange(blocks), "grads_pair_exchange"), "grads_pair_sum")
    received = _run_exchange(_chip_exchange(pair_sums), "grads_chip_exchange")
    g_mine = _chip_sum(list(pair_sums) + mlp_sums, list(received) + list(mlp_received))
    g_theirs = _run_exchange(_pair_share(g_mine), "grads_pair_share")

    g_small, d_small, nm_small, nv_small = _small_adamw(
        _small_allreduce(*small), [row(weights[n]) for n in SMALL_NAMES], [row(m_in[n]) for n in SMALL_NAMES],
        [row(v_in[n]) for n in SMALL_NAMES])

    g_big, d_big, nm_big, nv_big = _adamw_big([weights[n] for n in big], g_mine, g_theirs, [m_in[n] for n in big],
                                              [v_in[n] for n in big])

    grads, delta, new_m, new_v = {}, {}, {}, {}
    for i, n in enumerate(big):
        grads[n], delta[n], new_m[n], new_v[n] = g_big[i], d_big[i], nm_big[i], nv_big[i]
    for i, n in enumerate(SMALL_NAMES):
        shape = weights[n].shape
        grads[n], delta[n], new_m[n], new_v[n] = (g_small[i].reshape(shape), d_small[i].reshape(shape),
                                                  nm_small[i].reshape(shape), nv_small[i].reshape(shape))
    return (loss, grad_x[None], *[grads[n] for n in order_w], *[delta[n] for n in order_w], *[new_m[n] for n in order_w],
            *[new_v[n] for n in order_w])
```

```python
import jax
import jax.numpy as jnp
from jax import lax
from jax.experimental import pallas as pl
from jax.experimental.pallas import tpu as pltpu

F32 = jnp.float32
BF16 = jnp.bfloat16
MESH = pl.DeviceIdType.MESH

S = 2048
D = 1024
NFH, FHD = 8, 64
NPAIR = NFH // 2
NGH, GHD = 4, 128
DFOX = NFH * FHD
DGDN = NGH * GHD
CHUNK = 64
NCH = S // CHUNK
CONV_K = 4
DFF = 4 * D
EPS = 1e-6
DPROJ = 3600
LANES = 128
DPROJ_PAD = 3840
BLK_GDN = 12
BLK_GZ = 24
BLK_SMALL = 28
NCHIP = 4
NDEV = 8
VMEM_LIMIT = 56 * 1024 * 1024

ADAM_LR = 0.001
ADAM_B1 = 0.9
ADAM_B2 = 0.999
ADAM_EPS = 1e-08
ADAM_WD = 0.01
ADAM_STEP = 10


def _cparams(**kw):
    return pltpu.CompilerParams(vmem_limit_bytes=VMEM_LIMIT, **kw)


def _dn(ca, cb):
    return (((ca,), (cb,)), ((), ()))


def _dot(a, b, ca=1, cb=0):
    return lax.dot_general(a.astype(BF16), b.astype(BF16), _dn(ca, cb), preferred_element_type=F32)


def _hdot(a, b, ca=1, cb=0):
    return lax.dot_general(a.astype(F32), b.astype(F32), _dn(ca, cb), precision=lax.Precision.HIGHEST,
                           preferred_element_type=F32)


@jax.custom_vjp
def _mm_nn(a, b):
    return _dot(a, b, 1, 0)


def _mm_nn_fwd(a, b):
    return _dot(a, b, 1, 0), (a, b)


def _mm_nn_bwd(res, g):
    a, b = res
    return _dot(g, b, 1, 1), _dot(a, g, 0, 0)


_mm_nn.defvjp(_mm_nn_fwd, _mm_nn_bwd)


@jax.custom_vjp
def _mm_nt(a, b):
    return _dot(a, b, 1, 1)


def _mm_nt_fwd(a, b):
    return _dot(a, b, 1, 1), (a, b)


def _mm_nt_bwd(res, g):
    a, b = res
    return _dot(g, b, 1, 0), _dot(g, a, 0, 0)


_mm_nt.defvjp(_mm_nt_fwd, _mm_nt_bwd)


@jax.custom_vjp
def _saved_inverse(m, t_inv):
    del m
    return t_inv


def _saved_inverse_fwd(m, t_inv):
    del m
    return t_inv, t_inv


def _saved_inverse_bwd(t_inv, g):
    return -_hdot(_hdot(t_inv, g, 0, 0), t_inv, 1, 1), jnp.zeros_like(t_inv)


_saved_inverse.defvjp(_saved_inverse_fwd, _saved_inverse_bwd)


def _sigmoid(z):
    return 1.0 / (1.0 + jnp.exp(-z))


def _softplus(z):
    return jnp.maximum(z, 0.0) + jnp.log(1.0 + jnp.exp(-jnp.abs(z)))


def _silu(z):
    return z * _sigmoid(z)


def _rms_scale(x):
    return lax.rsqrt(jnp.mean(x * x, axis=-1, keepdims=True) + EPS)


def _rms_bwd(x, w, g):
    r = _rms_scale(x)
    gw = g * w
    dx = r * gw - x * (r * r * r) * jnp.mean(gw * x, axis=-1, keepdims=True)
    return dx, g * x * r


def _matmul(a, b, *, name, ta=False, tb=False, tm=512, tn=512, tk=512, out_dtypes=(F32,), b3=False, o3=False,
            extra=(), epilogue=None, exchange=None):
    m, k = (a.shape[1], a.shape[0]) if ta else a.shape
    if b3:
        n = b.shape[1] if tb else b.shape[0] * b.shape[2]
        kb = b.shape[0] * b.shape[2] if tb else b.shape[1]
    else:
        n, kb = (b.shape[0], b.shape[1]) if tb else (b.shape[1], b.shape[0])
    assert kb == k, (name, kb, k)
    tm, tn, tk = min(tm, m), min(tn, n), min(tk, k)
    assert m % tm == 0 and n % tn == 0 and k % tk == 0, (name, m, n, k, tm, tn, tk)
    nk = k // tk
    n_extra = len(extra)
    n_out = len(out_dtypes)
    grid = (m // tm, n // tn, nk)
    ex_in, ex_in_specs, ex_out_specs, ex_out_shape, ex_scratch = _hosted(exchange)

    def body(*refs):
        a_ref, b_ref = refs[0], refs[1]
        extra_refs = refs[2:2 + n_extra]
        first_out = 2 + n_extra + len(ex_in)
        out_refs = refs[first_out:first_out + n_out]
        ex_refs = refs[2 + n_extra:first_out] + refs[first_out + n_out:first_out + n_out + len(ex_out_shape)] + refs[-2:]
        step = [pl.program_id(d) for d in range(3)]

        if exchange is not None:
            @pl.when((step[0] == 0) & (step[1] == 0) & (step[2] == 0))
            def _():
                exchange.start(*exchange.split(ex_refs))

        def finish(acc):
            outs = (acc,) if epilogue is None else epilogue(acc, *[r[...] for r in extra_refs])
            for o_ref, val in zip(out_refs, outs):
                o_ref[...] = val.astype(o_ref.dtype)

        part = _dot(a_ref[...], b_ref[...], 0 if ta else 1, 1 if tb else 0)
        if nk == 1:
            finish(part)
        else:
            acc_ref = refs[first_out + n_out + len(ex_out_shape)]

            @pl.when(step[2] == 0)
            def _():
                acc_ref[...] = part

            @pl.when(step[2] > 0)
            def _():
                acc_ref[...] += part

            @pl.when(step[2] == nk - 1)
            def _():
                finish(acc_ref[...])

        if exchange is not None:
            @pl.when((step[0] == grid[0] - 1) & (step[1] == grid[1] - 1) & (step[2] == nk - 1))
            def _():
                exchange.finish(*exchange.split(ex_refs))

    a_spec = pl.BlockSpec((tk, tm), lambda i, j, kk: (kk, i)) if ta else pl.BlockSpec((tm, tk), lambda i, j, kk: (i, kk))
    if b3 and tb:
        assert b.shape[2] == tk
        b_spec = pl.BlockSpec((None, tn, tk), lambda i, j, kk: (kk, j, 0))
    elif b3:
        assert b.shape[2] == tn
        b_spec = pl.BlockSpec((None, tk, tn), lambda i, j, kk: (j, kk, 0))
    elif tb:
        b_spec = pl.BlockSpec((tn, tk), lambda i, j, kk: (j, kk))
    else:
        b_spec = pl.BlockSpec((tk, tn), lambda i, j, kk: (kk, j))
    tile = pl.BlockSpec((tm, tn), lambda i, j, kk: (i, j))
    out_specs = [tile] * n_out
    out_shape = [jax.ShapeDtypeStruct((m, n), dt) for dt in out_dtypes]
    if o3:
        out_specs[0] = pl.BlockSpec((None, tm, tn), lambda i, j, kk: (j, i, 0))
        out_shape[0] = jax.ShapeDtypeStruct((n // tn, m, tn), out_dtypes[0])
    res = pl.pallas_call(
        body, name=name, grid=grid,
        in_specs=[a_spec, b_spec] + [tile] * n_extra + ex_in_specs, out_specs=out_specs + ex_out_specs,
        out_shape=out_shape + ex_out_shape,
        scratch_shapes=([pltpu.VMEM((tm, tn), F32)] if nk > 1 else []) + ex_scratch,
        compiler_params=_cparams(),
    )(a, b, *extra, *ex_in)
    if exchange is not None:
        return (res[0] if n_out == 1 else res[:n_out]), res[n_out:]
    return res[0] if n_out == 1 else res


TR = 256


def _row_spec(cols):
    return pl.BlockSpec((TR, cols), lambda i: (i, 0))


def _vec_spec(cols):
    return pl.BlockSpec((1, cols), lambda i: (0, 0))


def _pre_norm(x, w):
    def body(x_ref, w_ref, h_ref):
        xv = x_ref[...]
        h_ref[...] = (xv * _rms_scale(xv) * w_ref[...]).astype(BF16)

    return pl.pallas_call(
        body, name="pre_norm", grid=(S // TR,), in_specs=[_row_spec(D), _vec_spec(D)], out_specs=_row_spec(D),
        out_shape=jax.ShapeDtypeStruct((S, D), BF16), compiler_params=_cparams(),
    )(x, w)


def _post_mix(x, mixed, w_post, w_pre_mlp):
    def body(x_ref, m_ref, wp_ref, wm_ref, x1_ref, h2_ref):
        mv = m_ref[...]
        x1 = x_ref[...] + mv * _rms_scale(mv) * wp_ref[...]
        x1_ref[...] = x1
        h2_ref[...] = (x1 * _rms_scale(x1) * wm_ref[...]).astype(BF16)

    return pl.pallas_call(
        body, name="post_mix", grid=(S // TR,),
        in_specs=[_row_spec(D), _row_spec(D), _vec_spec(D), _vec_spec(D)], out_specs=[_row_spec(D), _row_spec(D)],
        out_shape=[jax.ShapeDtypeStruct((S, D), F32), jax.ShapeDtypeStruct((S, D), BF16)], compiler_params=_cparams(),
    )(x, mixed, w_post, w_pre_mlp)


def _loss_head(x1, y, w_post_mlp, target):
    def body(x1_ref, y_ref, w_ref, t_ref, dx2_ref, dy_ref, dw_ref, loss_ref):
        i = pl.program_id(0)
        yv = y_ref[...]
        w = w_ref[...]
        x2 = x1_ref[...] + yv * _rms_scale(yv) * w
        err = x2 - t_ref[...]
        dx2 = err * (1.0 / D)
        dx2_ref[...] = dx2
        dy, dwt = _rms_bwd(yv, w, dx2)
        dy_ref[...] = dy.astype(BF16)

        @pl.when(i == 0)
        def _():
            dw_ref[...] = jnp.zeros_like(dw_ref)
            loss_ref[...] = jnp.zeros_like(loss_ref)

        dw_ref[...] += jnp.sum(dwt, axis=0, keepdims=True)
        part = 0.5 * jnp.sum(jnp.mean(err * err, axis=-1, keepdims=True), axis=0, keepdims=True)
        loss_ref[...] += jnp.broadcast_to(part, loss_ref.shape)

    return pl.pallas_call(
        body, name="loss_head", grid=(S // TR,),
        in_specs=[_row_spec(D), _row_spec(D), _vec_spec(D), _row_spec(D)],
        out_specs=[_row_spec(D), _row_spec(D), _vec_spec(D), _vec_spec(LANES)],
        out_shape=[jax.ShapeDtypeStruct((S, D), F32), jax.ShapeDtypeStruct((S, D), BF16),
                   jax.ShapeDtypeStruct((1, D), F32), jax.ShapeDtypeStruct((1, LANES), F32)],
        compiler_params=_cparams(),
    )(x1, y, w_post_mlp, target)


def _mid_bwd(dh2, x1, w_pre_mlp, dx2, mixed, w_post):
    def body(dh2_ref, x1_ref, wm_ref, dx2_ref, m_ref, wp_ref, dx1_ref, dm_ref, dwm_ref, dwp_ref):
        i = pl.program_id(0)
        dxa, dwm = _rms_bwd(x1_ref[...], wm_ref[...], dh2_ref[...])
        dx1 = dx2_ref[...] + dxa
        dx1_ref[...] = dx1
        dm, dwp = _rms_bwd(m_ref[...], wp_ref[...], dx1)
        dm_ref[...] = dm.astype(BF16)

        @pl.when(i == 0)
        def _():
            dwm_ref[...] = jnp.zeros_like(dwm_ref)
            dwp_ref[...] = jnp.zeros_like(dwp_ref)

        dwm_ref[...] += jnp.sum(dwm, axis=0, keepdims=True)
        dwp_ref[...] += jnp.sum(dwp, axis=0, keepdims=True)

    return pl.pallas_call(
        body, name="mid_bwd", grid=(S // TR,),
        in_specs=[_row_spec(D), _row_spec(D), _vec_spec(D), _row_spec(D), _row_spec(D), _vec_spec(D)],
        out_specs=[_row_spec(D), _row_spec(D), _vec_spec(D), _vec_spec(D)],
        out_shape=[jax.ShapeDtypeStruct((S, D), F32), jax.ShapeDtypeStruct((S, D), BF16),
                   jax.ShapeDtypeStruct((1, D), F32), jax.ShapeDtypeStruct((1, D), F32)],
        compiler_params=_cparams(),
    )(dh2, x1, w_pre_mlp, dx2, mixed, w_post)


def _pre_norm_bwd(dh, x, w, dx1):
    def body(dh_ref, x_ref, w_ref, dx1_ref, dx_ref, dw_ref):
        i = pl.program_id(0)
        dxa, dwt = _rms_bwd(x_ref[...], w_ref[...], dh_ref[...])
        dx_ref[...] = dx1_ref[...] + dxa

        @pl.when(i == 0)
        def _():
            dw_ref[...] = jnp.zeros_like(dw_ref)

        dw_ref[...] += jnp.sum(dwt, axis=0, keepdims=True)

    return pl.pallas_call(
        body, name="pre_norm_bwd", grid=(S // TR,),
        in_specs=[_row_spec(D), _row_spec(D), _vec_spec(D), _row_spec(D)], out_specs=[_row_spec(D), _vec_spec(D)],
        out_shape=[jax.ShapeDtypeStruct((S, D), F32), jax.ShapeDtypeStruct((1, D), F32)], compiler_params=_cparams(),
    )(dh, x, w, dx1)


BQ = 256
NQ = S // BQ
LANE_BETA, LANE_G = 8, 12


def _gate_lanes(shape):
    lane = lax.broadcasted_iota(jnp.int32, shape, 1)
    return lane < LANE_BETA, (lane >= LANE_BETA) & (lane < LANE_G), (lane >= LANE_G) & (lane < LANE_G + NGH)


def _gates(proj, bias_vec, alog_vec):
    def body(s_ref, b_ref, a_ref, o_ref, carry_ref):
        i = pl.program_id(0)

        @pl.when(i == 0)
        def _():
            carry_ref[...] = jnp.zeros_like(carry_ref)

        z = s_ref[...] + b_ref[...]
        tail = jnp.log(1.0 + jnp.exp(-jnp.abs(z)))
        sp = jnp.maximum(z, 0.0) + tail
        lf = jnp.minimum(z, 0.0) - tail
        r = lax.broadcasted_iota(jnp.int32, (BQ, BQ), 0)
        c = lax.broadcasted_iota(jnp.int32, (BQ, BQ), 1)
        tri = (c <= r).astype(F32)
        cum = _hdot(tri, lf) + carry_ref[...]
        carry_ref[...] = cum[BQ - 1:BQ, :]
        is_fox, is_beta, is_g = _gate_lanes(z.shape)
        o_ref[...] = jnp.where(is_fox, cum, jnp.where(is_beta, _sigmoid(z), jnp.where(is_g, -jnp.exp(a_ref[...]) * sp, 0.0)))

    return pl.pallas_call(
        body, name="gates", grid=(NQ,),
        in_specs=[pl.BlockSpec((BQ, LANES), lambda i: (i, BLK_SMALL)), _vec_spec(LANES), _vec_spec(LANES)],
        out_specs=pl.BlockSpec((BQ, LANES), lambda i: (i, 0)), out_shape=jax.ShapeDtypeStruct((S, LANES), F32),
        scratch_shapes=[pltpu.VMEM((1, LANES), F32)], compiler_params=_cparams(),
    )(proj, bias_vec, alog_vec)


def _gates_bwd(proj, bias_vec, alog_vec, dgates_gdn, dcum_fox, dproj):
    def body(s_ref, b_ref, a_ref, dg_ref, dc_ref, dproj_in, dproj_ref, red_ref, carry_ref):
        del dproj_in
        i = pl.program_id(0)

        @pl.when(i == 0)
        def _():
            carry_ref[...] = jnp.zeros_like(carry_ref)
            red_ref[...] = jnp.zeros_like(red_ref)

        z = s_ref[...] + b_ref[...]
        dg = dg_ref[...] + dc_ref[...]
        r = lax.broadcasted_iota(jnp.int32, (BQ, BQ), 0)
        c = lax.broadcasted_iota(jnp.int32, (BQ, BQ), 1)
        upper = (c >= r).astype(F32)
        dlf = _hdot(upper, dg) + carry_ref[...]
        carry_ref[...] = dlf[0:1, :]
        sig = _sigmoid(z)
        g_scale = -jnp.exp(a_ref[...])
        is_fox, is_beta, is_g = _gate_lanes(z.shape)
        ds = jnp.where(is_fox, dlf * (1.0 - sig), jnp.where(is_beta, dg * sig * (1.0 - sig), jnp.where(is_g, dg * g_scale * sig, 0.0)))
        dproj_ref[:, 0:LANES] = ds.astype(BF16)
        dproj_ref[:, LANES:2 * LANES] = jnp.zeros((BQ, LANES), BF16)
        dalog = jnp.where(is_g, dg * g_scale * _softplus(z), 0.0)
        sums = jnp.sum(ds, axis=0, keepdims=True)
        red_ref[0:1, :] += jnp.where(is_fox[0:1], sums, 0.0)
        red_ref[1:2, :] += pltpu.roll(jnp.where(is_g[0:1], sums, 0.0), LANES - LANE_G, 1)
        red_ref[2:3, :] += pltpu.roll(jnp.sum(dalog, axis=0, keepdims=True), LANES - LANE_G, 1)

    blk = pl.BlockSpec((BQ, LANES), lambda i: (NQ - 1 - i, 0))
    return pl.pallas_call(
        body, name="gates_bwd", grid=(NQ,),
        in_specs=[pl.BlockSpec((BQ, LANES), lambda i: (NQ - 1 - i, BLK_SMALL)), _vec_spec(LANES), _vec_spec(LANES), blk, blk,
                  pl.BlockSpec(memory_space=pl.ANY)],
        out_specs=[pl.BlockSpec((BQ, 2 * LANES), lambda i: (NQ - 1 - i, BLK_SMALL // 2)), pl.BlockSpec((8, LANES), lambda i: (0, 0))],
        out_shape=[jax.ShapeDtypeStruct((S, DPROJ_PAD), BF16), jax.ShapeDtypeStruct((8, LANES), F32)],
        input_output_aliases={5: 0},
        scratch_shapes=[pltpu.VMEM((1, LANES), F32)], compiler_params=_cparams(),
    )(proj, bias_vec, alog_vec, dgates_gdn, dcum_fox, dproj)


FOX_SCALE = FHD ** -0.5
FOX_PAIRS = 2
FOX_PAIRS_BWD = 2


def _head_mask(e):
    lane = lax.broadcasted_iota(jnp.int32, (1, LANES), 1)
    return (lane >= e * FHD) & (lane < (e + 1) * FHD)


def _lane_col(vals, index):
    lane = lax.broadcasted_iota(jnp.int32, vals.shape, 1)
    return jnp.sum(jnp.where(lane == index, vals, 0.0), axis=1, keepdims=True)


def _sublane_row(vals, index):
    row = lax.broadcasted_iota(jnp.int32, vals.shape, 0)
    return jnp.sum(jnp.where(row == index, vals, 0.0), axis=0, keepdims=True)


def _pair_cols(c0, c1):
    lane = lax.broadcasted_iota(jnp.int32, (c0.shape[0], 2), 1)
    return jnp.where(lane == 0, c0, c1)


def _split3(x):
    hi = x.astype(BF16).astype(F32)
    rest = x - hi
    mid = rest.astype(BF16).astype(F32)
    return hi, mid, (rest - mid).astype(BF16).astype(F32)


def _fox_operand(vals, e, cum, is_query):
    lane = lax.broadcasted_iota(jnp.int32, (1, LANES), 1)
    base = (1 - e) * FHD
    parts = _split3(cum)
    own = jnp.where(_head_mask(e), vals * FOX_SCALE if is_query else vals, 0.0)
    cum_at, ones_at = (base, base + 3) if is_query else (base + 3, base)
    sign = 1.0 if is_query else -1.0
    out = own + jnp.where((lane >= ones_at) & (lane < ones_at + 3), 1.0, 0.0)
    for i, part in enumerate(parts):
        out = out + jnp.where(lane == cum_at + i, sign * part, 0.0)
    return out.astype(BF16)


def _causal_block():
    return lax.broadcasted_iota(jnp.int32, (BQ, BQ), 1) <= lax.broadcasted_iota(jnp.int32, (BQ, BQ), 0)


def _head_rms(o, masks):
    o2 = o * o
    r = [lax.rsqrt(jnp.sum(jnp.where(mk, o2, 0.0), axis=1, keepdims=True) * (1.0 / FHD) + EPS) for mk in masks]
    return jnp.where(masks[0], r[0], r[1])


def _hosted(exchange):
    if exchange is None:
        return [], [], [], [], []
    return (exchange.inputs, [HBM] * len(exchange.inputs), [HBM] * len(exchange.out_shape), exchange.out_shape,
            exchange.sem_shapes())


def _fox_fwd(proj, gates, w2, exchange=None):
    ex_in, ex_in_specs, ex_out_specs, ex_out_shape, ex_scratch = _hosted(exchange)

    n_in = 3 * FOX_PAIRS + 2
    heads = [(pp, e) for pp in range(FOX_PAIRS) for e in range(2)]

    def body(*refs):
        qkv_refs, g_ref, w_ref = refs[:3 * FOX_PAIRS], refs[3 * FOX_PAIRS], refs[3 * FOX_PAIRS + 1]
        mix_ref, o_ref, lse_ref = refs[n_in + len(ex_in):n_in + 3 + len(ex_in)]
        ka_ref, vb_ref = refs[n_in + 3 + len(ex_in) + len(ex_out_shape):n_in + 5 + len(ex_in) + len(ex_out_shape)]
        ex_refs = refs[n_in:n_in + len(ex_in)] + refs[n_in + 3 + len(ex_in):n_in + 3 + len(ex_in) + len(ex_out_shape)] + refs[-2:]
        grp, qi = pl.program_id(0), pl.program_id(1)

        def head_index(pp, e):
            return 2 * (FOX_PAIRS * grp + pp) + e

        if exchange is not None:
            @pl.when((grp == 0) & (qi == 0))
            def _():
                exchange.start(*exchange.split(ex_refs))

        @pl.when(qi == 0)
        def _():
            gt = g_ref[...]
            for pp in range(FOX_PAIRS):
                kv = qkv_refs[3 * pp + 1][...]
                for e in range(2):
                    ka_ref[2 * pp + e] = _fox_operand(kv, e, _lane_col(gt, head_index(pp, e)), False)
                vb_ref[pp] = qkv_refs[3 * pp + 2][...].astype(BF16)

        masks = [_head_mask(0), _head_mask(1)]
        gt = g_ref[pl.ds(pl.multiple_of(qi * BQ, BQ), BQ), :]
        qs = [_fox_operand(qkv_refs[3 * pp][...], e, _lane_col(gt, head_index(pp, e)), True) for pp, e in heads]
        n = range(len(heads))

        def block(kj, carry, diagonal):
            rows = pl.ds(pl.multiple_of(kj * BQ, BQ), BQ)
            s = [_dot(qs[i], ka_ref[i, rows, :], 1, 1) for i in n]
            if diagonal:
                s = [jnp.where(_causal_block(), s[i], -jnp.inf) for i in n]
            m_new = [jnp.maximum(carry[i][0], jnp.max(s[i], axis=-1, keepdims=True)) for i in n]
            p = [jnp.exp(s[i] - m_new[i]) for i in n]
            alpha = [jnp.exp(carry[i][0] - m_new[i]) for i in n]
            l_new = [alpha[i] * carry[i][1] + jnp.sum(p[i], axis=-1, keepdims=True) for i in n]
            pv = [_dot(p[i], vb_ref[heads[i][0], rows, :]) for i in n]
            return tuple((m_new[i], l_new[i], alpha[i] * carry[i][2] + pv[i]) for i in n)

        one = (jnp.full((BQ, 1), -jnp.inf, F32), jnp.zeros((BQ, 1), F32), jnp.zeros((BQ, LANES), F32))
        below = lax.fori_loop(0, qi, lambda kj, carry: block(kj, carry, False), (one,) * len(heads))
        done = block(qi, below, True)
        for pp in range(FOX_PAIRS):
            (m0, l0, a0), (m1, l1, a1) = done[2 * pp], done[2 * pp + 1]
            o = jnp.where(masks[0], a0 / l0, a1 / l1)
            cols = slice(pp * LANES, (pp + 1) * LANES)
            o_ref[:, cols] = o
            mix_ref[:, cols] = (o * _head_rms(o, masks) * w_ref[...]).astype(BF16)
            lse_ref[pp] = _pair_cols(m0 + jnp.log(l0), m1 + jnp.log(l1))

        if exchange is not None:
            @pl.when((grp == NPAIR // FOX_PAIRS - 1) & (qi == NQ - 1))
            def _():
                exchange.finish(*exchange.split(ex_refs))

    qkv_specs = []
    for pp in range(FOX_PAIRS):
        qkv_specs.append(pl.BlockSpec((BQ, LANES), lambda g, i, pp=pp: (i, 3 * (FOX_PAIRS * g + pp))))
        qkv_specs.append(pl.BlockSpec((S, LANES), lambda g, i, pp=pp: (0, 3 * (FOX_PAIRS * g + pp) + 1)))
        qkv_specs.append(pl.BlockSpec((S, LANES), lambda g, i, pp=pp: (0, 3 * (FOX_PAIRS * g + pp) + 2)))
    blk = pl.BlockSpec((BQ, FOX_PAIRS * LANES), lambda g, i: (i, g))
    res = pl.pallas_call(
        body, name="fox_fwd", grid=(NPAIR // FOX_PAIRS, NQ),
        in_specs=qkv_specs + [pl.BlockSpec((S, LANES), lambda g, i: (0, 0)), pl.BlockSpec((1, LANES), lambda g, i: (0, 0))]
        + ex_in_specs,
        out_specs=[blk, blk, pl.BlockSpec((FOX_PAIRS, BQ, 2), lambda g, i: (g, i, 0))] + ex_out_specs,
        out_shape=[jax.ShapeDtypeStruct((S, D), BF16), jax.ShapeDtypeStruct((S, DFOX), F32),
                   jax.ShapeDtypeStruct((NPAIR, S, 2), F32)] + ex_out_shape,
        scratch_shapes=[pltpu.VMEM((2 * FOX_PAIRS, S, LANES), BF16), pltpu.VMEM((FOX_PAIRS, S, LANES), BF16)] + ex_scratch,
        compiler_params=_cparams(),
    )(*([proj] * (3 * FOX_PAIRS)), gates, w2, *ex_in)
    return res[0], res[1], res[2], res[3:]


def _fox_norm_bwd(o, dmix, w2):
    def body(o_ref, g_ref, w_ref, do_ref, dl_ref, dw_ref):
        hp, qi = pl.program_id(0), pl.program_id(1)
        masks = [_head_mask(0), _head_mask(1)]
        ov = o_ref[...]
        g = g_ref[...]
        r = _head_rms(ov, masks)
        gw = g * w_ref[...]
        gwo = gw * ov
        mean = [jnp.sum(jnp.where(mk, gwo, 0.0), axis=1, keepdims=True) * (1.0 / FHD) for mk in masks]
        do = r * gw - ov * (r * r * r) * jnp.where(masks[0], mean[0], mean[1])
        do_ref[...] = do.astype(BF16)
        doo = do * ov
        dl_ref[...] = _pair_cols(*[jnp.sum(jnp.where(mk, doo, 0.0), axis=1, keepdims=True) for mk in masks])

        @pl.when((hp == 0) & (qi == 0))
        def _():
            dw_ref[...] = jnp.zeros_like(dw_ref)

        dw_ref[...] += jnp.sum(g * ov * r, axis=0, keepdims=True)

        @pl.when((hp == NPAIR - 1) & (qi == NQ - 1))
        def _():
            dw = dw_ref[...]
            dw_ref[...] = dw + pltpu.roll(dw, FHD, 1)

    blk = pl.BlockSpec((BQ, LANES), lambda hp, i: (i, hp))
    vec = pl.BlockSpec((1, LANES), lambda hp, i: (0, 0))
    return pl.pallas_call(
        body, name="fox_norm_bwd", grid=(NPAIR, NQ), in_specs=[blk, blk, vec],
        out_specs=[blk, pl.BlockSpec((None, BQ, 2), lambda hp, i: (hp, i, 0)), vec],
        out_shape=[jax.ShapeDtypeStruct((S, DFOX), BF16), jax.ShapeDtypeStruct((NPAIR, S, 2), F32),
                   jax.ShapeDtypeStruct((1, LANES), F32)],
        compiler_params=_cparams(),
    )(o, dmix, w2)


def _fox_bwd(proj, do, gates, lse, delta, exchange=None):
    ex_in, ex_in_specs, ex_out_specs, ex_out_shape, ex_scratch = _hosted(exchange)

    pg = FOX_PAIRS_BWD
    n_in = 3 * pg + 4
    heads = [(pp, e) for pp in range(pg) for e in range(2)]

    def body(*refs):
        qkv_refs = refs[:3 * pg]
        do_ref, g_ref, lse_ref, dl_ref = refs[3 * pg:n_in]
        dproj_ref, dc_ref = refs[n_in + len(ex_in):n_in + 2 + len(ex_in)]
        qa_ref, dq_ref = refs[n_in + 2 + len(ex_in) + len(ex_out_shape):n_in + 4 + len(ex_in) + len(ex_out_shape)]
        ex_refs = refs[n_in:n_in + len(ex_in)] + refs[n_in + 2 + len(ex_in):n_in + 2 + len(ex_in) + len(ex_out_shape)] + refs[-2:]
        grp, kj = pl.program_id(0), pl.program_id(1)

        def head_index(pp, e):
            return 2 * (pg * grp + pp) + e

        if exchange is not None:
            @pl.when((grp == 0) & (kj == 0))
            def _():
                exchange.start(*exchange.split(ex_refs))

        @pl.when(kj == 0)
        def _():
            gt = g_ref[...]
            for pp in range(pg):
                qv = qkv_refs[3 * pp][...]
                for e in range(2):
                    qa_ref[2 * pp + e] = _fox_operand(qv, e, _lane_col(gt, head_index(pp, e)), True)
            dq_ref[...] = jnp.zeros_like(dq_ref)

        @pl.when((grp == 0) & (kj == 0))
        def _():
            dc_ref[...] = jnp.zeros_like(dc_ref)

        masks = [_head_mask(0), _head_mask(1)]
        krows = pl.ds(pl.multiple_of(kj * BQ, BQ), BQ)
        gk = g_ref[krows, :]
        kas = [_fox_operand(qkv_refs[3 * pp + 1][...], e, _lane_col(gk, head_index(pp, e)), False) for pp, e in heads]
        vbs = [qkv_refs[3 * pp + 2][...].astype(BF16) for pp in range(pg)]
        lane = lax.broadcasted_iota(jnp.int32, (BQ, LANES), 1)
        n = range(len(heads))

        def block(qi, carry, diagonal):
            dks, dvs, css = carry
            rows = pl.ds(pl.multiple_of(qi * BQ, BQ), BQ)
            qa = [qa_ref[i, rows, :] for i in n]
            s = [_dot(qa[i], kas[i], 1, 1) for i in n]
            if diagonal:
                s = [jnp.where(_causal_block(), s[i], -jnp.inf) for i in n]
            dov = [do_ref[rows, pp * LANES:(pp + 1) * LANES] for pp in range(pg)]
            doe = [jnp.where(masks[e], dov[pp], jnp.zeros_like(dov[pp])) for pp, e in heads]
            lse2 = [lse_ref[pp, rows, :] for pp in range(pg)]
            dl2 = [dl_ref[pp, rows, :] for pp in range(pg)]
            p = [jnp.exp(s[i] - _lane_col(lse2[heads[i][0]], heads[i][1])) for i in n]
            dp = [_dot(doe[i], vbs[heads[i][0]], 1, 1) for i in n]
            ds = [p[i] * (dp[i] - _lane_col(dl2[heads[i][0]], heads[i][1])) for i in n]
            dv_part = [_dot(p[i], doe[i], 0, 0) for i in n]
            dk_part = [_dot(ds[i], jnp.where(masks[heads[i][1]], qa[i], jnp.zeros_like(qa[i])), 0, 0) for i in n]
            dq_part = [jnp.where(masks[heads[i][1]], _dot(ds[i], kas[i]), 0.0) for i in n]
            css = tuple(css[i] + jnp.sum(ds[i], axis=0, keepdims=True) for i in n)
            dc = jnp.zeros((BQ, LANES), F32)
            for i in n:
                dc = dc + jnp.where(lane == head_index(*heads[i]), jnp.sum(ds[i], axis=1, keepdims=True), 0.0)
            for pp in range(pg):
                dq_ref[pp, rows, :] += (dq_part[2 * pp] + dq_part[2 * pp + 1]) * FOX_SCALE
            dc_ref[rows, :] += dc
            dks = tuple(dks[pp] + dk_part[2 * pp] + dk_part[2 * pp + 1] for pp in range(pg))
            dvs = tuple(dvs[pp] + dv_part[2 * pp] + dv_part[2 * pp + 1] for pp in range(pg))
            return dks, dvs, css

        zero = jnp.zeros((BQ, LANES), F32)
        first = block(kj, ((zero,) * pg, (zero,) * pg, (jnp.zeros((1, BQ), F32),) * len(heads)), True)
        dks, dvs, css = lax.fori_loop(kj + 1, NQ, lambda qi, carry: block(qi, carry, False), first)
        r = lax.broadcasted_iota(jnp.int32, (BQ, BQ), 0)
        c = lax.broadcasted_iota(jnp.int32, (BQ, BQ), 1)
        dcol = jnp.zeros((BQ, LANES), F32)
        for i in n:
            col = jnp.sum(jnp.where(r == c, css[i], 0.0), axis=1, keepdims=True)
            dcol = dcol + jnp.where(lane == head_index(*heads[i]), col, 0.0)
        dc_ref[krows, :] -= dcol
        for pp in range(pg):
            base = 3 * pp * LANES
            dproj_ref[krows, base + LANES:base + 2 * LANES] = dks[pp].astype(BF16)
            dproj_ref[krows, base + 2 * LANES:base + 3 * LANES] = dvs[pp].astype(BF16)

        @pl.when(kj == NQ - 1)
        def _():
            for pp in range(pg):
                dproj_ref[:, 3 * pp * LANES:(3 * pp + 1) * LANES] = dq_ref[pp].astype(BF16)

        if exchange is not None:
            @pl.when((grp == NPAIR // pg - 1) & (kj == NQ - 1))
            def _():
                exchange.finish(*exchange.split(ex_refs))

    qkv_specs = []
    for pp in range(pg):
        qkv_specs.append(pl.BlockSpec((S, LANES), lambda g, j, pp=pp: (0, 3 * (pg * g + pp))))
        qkv_specs.append(pl.BlockSpec((BQ, LANES), lambda g, j, pp=pp: (j, 3 * (pg * g + pp) + 1)))
        qkv_specs.append(pl.BlockSpec((BQ, LANES), lambda g, j, pp=pp: (j, 3 * (pg * g + pp) + 2)))
    pair = pl.BlockSpec((pg, S, 2), lambda g, j: (g, 0, 0))
    res = pl.pallas_call(
        body, name="fox_bwd", grid=(NPAIR // pg, NQ),
        in_specs=qkv_specs + [pl.BlockSpec((S, pg * LANES), lambda g, j: (0, g)), pl.BlockSpec((S, LANES), lambda g, j: (0, 0)),
                              pair, pair] + ex_in_specs,
        out_specs=[pl.BlockSpec((S, 3 * pg * LANES), lambda g, j: (0, g)), pl.BlockSpec((S, LANES), lambda g, j: (0, 0))]
        + ex_out_specs,
        out_shape=[jax.ShapeDtypeStruct((S, DPROJ_PAD), BF16), jax.ShapeDtypeStruct((S, LANES), F32)] + ex_out_shape,
        scratch_shapes=[pltpu.VMEM((2 * pg, S, LANES), BF16), pltpu.VMEM((pg, S, LANES), F32)] + ex_scratch,
        compiler_params=_cparams(),
    )(*([proj] * (3 * pg)), do, gates, lse, delta, *ex_in)
    return res[0], res[1], res[2:]


NQKV = 3 * NGH
GDN_QSCALE = GHD ** -0.5


def _shift_down(x, s):
    if s == 0:
        return x
    row = lax.broadcasted_iota(jnp.int32, x.shape, 0)
    return jnp.where(row >= s, pltpu.roll(x, s, 0), 0.0)


def _shift_up(x, s):
    if s == 0:
        return x
    n = x.shape[0]
    row = lax.broadcasted_iota(jnp.int32, x.shape, 0)
    return jnp.where(row < n - s, pltpu.roll(x, n - s, 0), 0.0)


def _conv_pre(xv, wv):
    pre = xv * wv[CONV_K - 1:CONV_K, :]
    for j in range(CONV_K - 1):
        pre = pre + _shift_down(xv, CONV_K - 1 - j) * wv[j:j + 1, :]
    return pre


def _l2_factors(b):
    return b < 2 * NGH, jnp.where(b < NGH, GDN_QSCALE, 1.0)


def _gdn_pre(proj, conv_w):
    def body(x_ref, w_ref, o_ref):
        b = pl.program_id(0)
        c = _silu(_conv_pre(x_ref[...], w_ref[...]))
        normed, scale = _l2_factors(b)
        rs = lax.rsqrt(jnp.sum(c * c, axis=-1, keepdims=True) + EPS)
        o_ref[...] = c * jnp.where(normed, rs, 1.0) * scale

    return pl.pallas_call(
        body, name="gdn_pre", grid=(NQKV,),
        in_specs=[pl.BlockSpec((S, GHD), lambda b: (0, BLK_GDN + b)), pl.BlockSpec((CONV_K, GHD), lambda b: (0, b))],
        out_specs=pl.BlockSpec((S, GHD), lambda b: (0, b)),
        out_shape=jax.ShapeDtypeStruct((S, NQKV * GHD), F32), compiler_params=_cparams(),
    )(proj, conv_w)


def _gdn_pre_bwd(proj, conv_w, dqkv, dproj):
    def body(x_ref, w_ref, dy_ref, dproj_in, dx_ref, dw_ref):
        del dproj_in
        b = pl.program_id(0)
        xv = x_ref[...]
        wv = w_ref[...]
        pre = _conv_pre(xv, wv)
        sig = _sigmoid(pre)
        c = pre * sig
        normed, scale = _l2_factors(b)
        g = dy_ref[...] * scale
        rs = lax.rsqrt(jnp.sum(c * c, axis=-1, keepdims=True) + EPS)
        dc_n = rs * g - c * (rs * rs * rs) * jnp.sum(g * c, axis=-1, keepdims=True)
        dc = jnp.where(normed, dc_n, g)
        dpre = dc * sig * (1.0 + pre * (1.0 - sig))
        dx = dpre * wv[CONV_K - 1:CONV_K, :]
        for j in range(CONV_K - 1):
            dx = dx + _shift_up(dpre, CONV_K - 1 - j) * wv[j:j + 1, :]
        dx_ref[...] = dx.astype(BF16)
        for j in range(CONV_K):
            dw_ref[j:j + 1, :] = jnp.sum(dpre * _shift_down(xv, CONV_K - 1 - j), axis=0, keepdims=True)

    return pl.pallas_call(
        body, name="gdn_pre_bwd", grid=(NQKV,),
        in_specs=[pl.BlockSpec((S, GHD), lambda b: (0, BLK_GDN + b)), pl.BlockSpec((CONV_K, GHD), lambda b: (0, b)),
                  pl.BlockSpec((None, S, GHD), lambda b: (b // NGH, 0, b % NGH)), pl.BlockSpec(memory_space=pl.ANY)],
        out_specs=[pl.BlockSpec((S, GHD), lambda b: (0, BLK_GDN + b)), pl.BlockSpec((CONV_K, GHD), lambda b: (0, b))],
        out_shape=[jax.ShapeDtypeStruct((S, DPROJ_PAD), BF16), jax.ShapeDtypeStruct((CONV_K, NQKV * GHD), F32)],
        input_output_aliases={3: 0}, compiler_params=_cparams(),
    )(proj, conv_w, dqkv, dproj)


CB = 4
NCB = NCH // CB


def _chunk_prep(qs, ks, vs, gcols, bcols, t_saved=None):
    n = range(len(qs))
    r = lax.broadcasted_iota(jnp.int32, (CHUNK, CHUNK), 0)
    c = lax.broadcasted_iota(jnp.int32, (CHUNK, CHUNK), 1)
    incl = c <= r
    eye = (r == c).astype(F32)
    grow = [jnp.sum(gcols[i] * eye, axis=0, keepdims=True) for i in n]
    gc_col = [jnp.sum(jnp.where(incl, grow[i], 0.0), axis=1, keepdims=True) for i in n]
    gc_row = [jnp.sum(jnp.where(r <= c, gcols[i], 0.0), axis=0, keepdims=True) for i in n]
    decay = [jnp.exp(jnp.where(incl, gc_col[i] - gc_row[i], -jnp.inf)) for i in n]
    kb = [ks[i] * bcols[i] for i in n]
    vb = [vs[i] * bcols[i] for i in n]
    kk = [_mm_nt(kb[i], ks[i]) for i in n]
    m = [jnp.where(c < r, kk[i] * decay[i], 0.0) for i in n]
    if t_saved is None:
        t_inv = [eye - m[i] for i in n]
        p = [_hdot(m[i], m[i]) for i in n]
        for step in range(5):
            t_inv = [t_inv[i] + _hdot(t_inv[i], p[i]) for i in n]
            if step < 4:
                p = [_hdot(p[i], p[i]) for i in n]
    else:
        t_inv = [_saved_inverse(m[i], t_saved[i]) for i in n]
    egc = [jnp.exp(gc_col[i]) for i in n]
    u = [_mm_nn(t_inv[i], vb[i]) for i in n]
    w = [_mm_nn(t_inv[i], kb[i] * egc[i]) for i in n]
    qk = [_mm_nt(qs[i], ks[i]) for i in n]
    gc_last = [gc_col[i][CHUNK - 1:CHUNK, :] for i in n]
    return [(u[i], w[i], qk[i] * decay[i], qs[i] * egc[i], ks[i] * jnp.exp(gc_last[i] - gc_col[i]), jnp.exp(gc_last[i]),
             t_inv[i]) for i in n]


def _prep_specs():
    rows = CB * CHUNK
    qs = pl.BlockSpec((rows, GHD), lambda i, h: (i, h))
    ks = pl.BlockSpec((rows, GHD), lambda i, h: (i, NGH + h))
    vs = pl.BlockSpec((rows, GHD), lambda i, h: (i, 2 * NGH + h))
    gs = pl.BlockSpec((rows, LANES), lambda i, h: (i, 0))
    a_s = pl.BlockSpec((None, rows, CHUNK), lambda i, h: (h, i, 0))
    gl_s = pl.BlockSpec((None, CB, 1, LANES), lambda i, h: (h, i, 0, 0))
    return qs, ks, vs, gs, a_s, gl_s


def _gdn_prep(qkv, gates, exchange=None):
    ex_in, ex_in_specs, ex_out_specs, ex_out_shape, ex_scratch = _hosted(exchange)

    def body(*refs):
        q_ref, k_ref, v_ref, g_ref = refs[:4]
        u_ref, w_ref, qd_ref, kd_ref, a_ref, gl_ref, t_ref = refs[4 + len(ex_in):11 + len(ex_in)]
        ex_refs = refs[4:4 + len(ex_in)] + refs[11 + len(ex_in):]
        h = pl.program_id(1)

        if exchange is not None:
            @pl.when((pl.program_id(0) == 0) & (h == 0))
            def _():
                exchange.start(*exchange.split(ex_refs))

        chunks = [pl.ds(cidx * CHUNK, CHUNK) for cidx in range(CB)]
        gts = [g_ref[rows, :] for rows in chunks]
        outs = _chunk_prep([q_ref[rows, :] for rows in chunks], [k_ref[rows, :] for rows in chunks],
                           [v_ref[rows, :] for rows in chunks], [_lane_col(gt, LANE_G + h) for gt in gts],
                           [_lane_col(gt, LANE_BETA + h) for gt in gts])
        for cidx, rows in enumerate(chunks):
            u, w, a, qd, kd, gl, t_inv = outs[cidx]
            u_ref[rows, :] = u
            w_ref[rows, :] = w
            qd_ref[rows, :] = qd
            kd_ref[rows, :] = kd
            a_ref[rows, :] = a
            t_ref[rows, :] = t_inv
            gl_ref[cidx] = jnp.broadcast_to(gl, (1, LANES))

        if exchange is not None:
            @pl.when((pl.program_id(0) == NCB - 1) & (h == NGH - 1))
            def _():
                exchange.finish(*exchange.split(ex_refs))

    qs, ks, vs, gs, a_s, gl_s = _prep_specs()
    tok = jax.ShapeDtypeStruct((S, DGDN), F32)
    sq = jax.ShapeDtypeStruct((NGH, S, CHUNK), F32)
    res = pl.pallas_call(
        body, name="gdn_prep", grid=(NCB, NGH), in_specs=[qs, ks, vs, gs] + ex_in_specs,
        out_specs=[qs, qs, qs, qs, a_s, gl_s, a_s] + ex_out_specs,
        out_shape=[tok, tok, tok, tok, sq, jax.ShapeDtypeStruct((NGH, NCH, 1, LANES), F32), sq] + ex_out_shape,
        scratch_shapes=ex_scratch, compiler_params=_cparams(),
    )(qkv, qkv, qkv, gates, *ex_in)
    return res[:7], res[7:]


def _gdn_prep_bwd(qkv, gates, t_inv, du, dw, dqd, dkd, da, dgl):
    def body(q_ref, k_ref, v_ref, g_ref, t_ref, du_ref, dw_ref, dqd_ref, dkd_ref, da_ref, dgl_ref, dqkv_ref, dg_ref):
        h = pl.program_id(1)

        @pl.when(h == 0)
        def _():
            dg_ref[...] = jnp.zeros_like(dg_ref)

        lane = lax.broadcasted_iota(jnp.int32, (CHUNK, LANES), 1)
        chunks = [pl.ds(cidx * CHUNK, CHUNK) for cidx in range(CB)]
        gts = [g_ref[rows, :] for rows in chunks]
        t_saved = [t_ref[rows, :] for rows in chunks]
        _, vjp = jax.vjp(lambda *args: [o[:6] for o in _chunk_prep(*args, t_saved=t_saved)],
                         [q_ref[rows, :] for rows in chunks], [k_ref[rows, :] for rows in chunks],
                         [v_ref[rows, :] for rows in chunks], [_lane_col(gt, LANE_G + h) for gt in gts],
                         [_lane_col(gt, LANE_BETA + h) for gt in gts])
        dqs, dks, dvs, dgcs, dbcs = vjp([(du_ref[rows, :], dw_ref[rows, :], da_ref[rows, :], dqd_ref[rows, :],
                                          dkd_ref[rows, :], dgl_ref[cidx][:, 0:1]) for cidx, rows in enumerate(chunks)])
        for cidx, rows in enumerate(chunks):
            dq, dk, dv, dgc, dbc = dqs[cidx], dks[cidx], dvs[cidx], dgcs[cidx], dbcs[cidx]
            dqkv_ref[0, rows, :] = dq
            dqkv_ref[1, rows, :] = dk
            dqkv_ref[2, rows, :] = dv
            dg_ref[rows, :] += jnp.where(lane == LANE_G + h, dgc, 0.0) + jnp.where(lane == LANE_BETA + h, dbc, 0.0)

    qs, ks, vs, gs, a_s, gl_s = _prep_specs()
    return pl.pallas_call(
        body, name="gdn_prep_bwd", grid=(NCB, NGH), in_specs=[qs, ks, vs, gs, a_s, qs, qs, qs, qs, a_s, gl_s],
        out_specs=[pl.BlockSpec((3, CB * CHUNK, GHD), lambda i, h: (0, i, h)), gs],
        out_shape=[jax.ShapeDtypeStruct((3, S, DGDN), F32), jax.ShapeDtypeStruct((S, LANES), F32)],
        compiler_params=_cparams(),
    )(qkv, qkv, qkv, gates, t_inv, du, dw, dqd, dkd, da, dgl)


def _scan_specs():
    hs = pl.BlockSpec((S, GHD), lambda h: (0, h))
    a_s = pl.BlockSpec((None, S, CHUNK), lambda h: (h, 0, 0))
    gl_s = pl.BlockSpec((None, NCH, 1, LANES), lambda h: (h, 0, 0, 0))
    st_s = pl.BlockSpec((None, NCH, GHD, GHD), lambda h: (h, 0, 0, 0))
    gz_s = pl.BlockSpec((S, GHD), lambda h: (0, BLK_GZ + h))
    mix_s = pl.BlockSpec((S, GHD), lambda h: (0, NPAIR + h))
    return hs, a_s, gl_s, st_s, gz_s, mix_s


def _gdn_scan(u, w, qd, kd, a, gl, proj, w_norm, mix):
    def body(u_ref, w_ref, qd_ref, kd_ref, a_ref, gl_ref, z_ref, wn_ref, mix_in, mix_ref, o_ref, st_ref):
        del mix_in

        def step(ci, state):
            rows = pl.ds(pl.multiple_of(ci * CHUNK, CHUNK), CHUNK)
            st_ref[ci] = state
            vn = u_ref[rows, :] - _dot(w_ref[rows, :], state)
            o_ref[rows, :] = _dot(qd_ref[rows, :], state) + _dot(a_ref[rows, :], vn)
            return state * gl_ref[ci] + _dot(kd_ref[rows, :], vn, 0, 0)

        lax.fori_loop(0, NCH, step, jnp.zeros((GHD, GHD), F32))
        ov = o_ref[...]
        mix_ref[...] = (ov * _rms_scale(ov) * wn_ref[...] * _silu(z_ref[...])).astype(BF16)

    hs, a_s, gl_s, st_s, gz_s, mix_s = _scan_specs()
    return pl.pallas_call(
        body, name="gdn_scan", grid=(NGH,),
        in_specs=[hs, hs, hs, hs, a_s, gl_s, gz_s, pl.BlockSpec((1, GHD), lambda h: (0, 0)), pl.BlockSpec(memory_space=pl.ANY)],
        out_specs=[mix_s, hs, st_s],
        out_shape=[jax.ShapeDtypeStruct((S, D), BF16), jax.ShapeDtypeStruct((S, DGDN), F32),
                   jax.ShapeDtypeStruct((NGH, NCH, GHD, GHD), F32)],
        input_output_aliases={8: 0}, compiler_params=_cparams(),
    )(u, w, qd, kd, a, gl, proj, w_norm, mix)


def _gdn_scan_bwd(dmix, o, proj, w_norm, u, w, qd, kd, a, gl, states, dproj):
    def body(dy_ref, o_ref, z_ref, wn_ref, u_ref, w_ref, qd_ref, kd_ref, a_ref, gl_ref, st_ref, dproj_in,
             dz_ref, du_ref, dw_ref, dqd_ref, dkd_ref, da_ref, dgl_ref, dwn_ref, do_ref):
        del dproj_in
        h = pl.program_id(0)
        ov = o_ref[...]
        zv = z_ref[...]
        wn = wn_ref[...]
        g = dy_ref[...]
        sig = _sigmoid(zv)
        dz_ref[...] = (g * (ov * _rms_scale(ov) * wn) * sig * (1.0 + zv * (1.0 - sig))).astype(BF16)
        do, dwt = _rms_bwd(ov, wn, g * zv * sig)
        do_ref[...] = do

        @pl.when(h == 0)
        def _():
            dwn_ref[...] = jnp.zeros_like(dwn_ref)

        dwn_ref[...] += jnp.sum(dwt, axis=0, keepdims=True)

        def step(t, dstate):
            ci = NCH - 1 - t
            rows = pl.ds(pl.multiple_of(ci * CHUNK, CHUNK), CHUNK)
            state = st_ref[ci]
            dov = do_ref[rows, :]
            wv = w_ref[rows, :]
            kdv = kd_ref[rows, :]
            vn = u_ref[rows, :] - _dot(wv, state)
            dvn = _dot(a_ref[rows, :], dov, 0, 0) + _dot(kdv, dstate)
            da_ref[rows, :] = _dot(dov, vn, 1, 1)
            dqd_ref[rows, :] = _dot(dov, state, 1, 1)
            dkd_ref[rows, :] = _dot(vn, dstate, 1, 1)
            dgl = jnp.sum(jnp.sum(dstate * state, axis=1, keepdims=True), axis=0, keepdims=True)
            dgl_ref[ci] = jnp.broadcast_to(dgl, (1, LANES))
            du_ref[rows, :] = dvn
            dw_ref[rows, :] = -_dot(dvn, state, 1, 1)
            return dstate * gl_ref[ci] + _dot(qd_ref[rows, :], dov, 0, 0) - _dot(wv, dvn, 0, 0)

        lax.fori_loop(0, NCH, step, jnp.zeros((GHD, GHD), F32))

    hs, a_s, gl_s, st_s, gz_s, mix_s = _scan_specs()
    vec = pl.BlockSpec((1, GHD), lambda h: (0, 0))
    tok = jax.ShapeDtypeStruct((S, DGDN), F32)
    return pl.pallas_call(
        body, name="gdn_scan_bwd", grid=(NGH,),
        in_specs=[mix_s, hs, gz_s, vec, hs, hs, hs, hs, a_s, gl_s, st_s, pl.BlockSpec(memory_space=pl.ANY)],
        out_specs=[gz_s, hs, hs, hs, hs, a_s, gl_s, vec],
        out_shape=[jax.ShapeDtypeStruct((S, DPROJ_PAD), BF16), tok, tok, tok, tok,
                   jax.ShapeDtypeStruct((NGH, S, CHUNK), F32), jax.ShapeDtypeStruct((NGH, NCH, 1, LANES), F32),
                   jax.ShapeDtypeStruct((1, GHD), F32)],
        input_output_aliases={11: 0}, scratch_shapes=[pltpu.VMEM((S, GHD), F32)], compiler_params=_cparams(),
    )(dmix, o, proj, w_norm, u, w, qd, kd, a, gl, states, dproj)


def _place():
    return lax.axis_index("x"), lax.axis_index("y"), lax.axis_index("c")


def _other_chips(x, y):
    return [(1 - x, y), (x, 1 - y), (1 - x, 1 - y)]


HBM = pl.BlockSpec(memory_space=pltpu.HBM)
VMEM = pl.BlockSpec(memory_space=pltpu.VMEM)


def _half_rows(ref_or_rows, half):
    rows = ref_or_rows // 2
    return pl.ds(pl.multiple_of(half * rows, rows), rows)


class _Exchange:
    def __init__(self, inputs, out_shape, n_sems, start, finish):
        self.inputs, self.out_shape, self.n_sems, self.start, self.finish = inputs, out_shape, n_sems, start, finish

    def sem_shapes(self):
        return [pltpu.SemaphoreType.DMA((self.n_sems,)), pltpu.SemaphoreType.DMA((self.n_sems,))]

    def split(self, refs):
        n_in, n_out = len(self.inputs), len(self.out_shape)
        return refs[:n_in], refs[n_in:n_in + n_out], refs[n_in + n_out], refs[n_in + n_out + 1]


def _run_exchange(ex, name):
    def body(*refs):
        parts = ex.split(refs)
        ex.start(*parts)
        ex.finish(*parts)

    return pl.pallas_call(
        body, name=name, in_specs=[HBM] * len(ex.inputs), out_specs=[HBM] * len(ex.out_shape), out_shape=ex.out_shape,
        scratch_shapes=ex.sem_shapes(), compiler_params=_cparams(),
    )(*ex.inputs)


def _allgather_exchange(shards, whole=()):
    n, nw = len(shards), len(whole)

    def plan(src, outs, send_sems, recv_sems):
        x, y, c = _place()
        chips = _other_chips(x, y)
        chip_ids = [2 * ch[0] + ch[1] for ch in chips]

        def copy(a, k, chip_index, half, to, from_src):
            rows = _half_rows(src[a].shape[0], half)
            dst = outs[a].at[chip_index, rows]
            return pltpu.make_async_remote_copy(
                src_ref=src[a].at[rows] if from_src else dst, dst_ref=dst, send_sem=send_sems.at[6 * a + k],
                recv_sem=recv_sems.at[6 * a + k], device_id=to, device_id_type=MESH)

        def whole_copy(b, k, chip_index, to):
            return pltpu.make_async_remote_copy(
                src_ref=src[n + b], dst_ref=outs[n + b].at[chip_index], send_sem=send_sems.at[6 * n + 3 * b + k],
                recv_sem=recv_sems.at[6 * n + 3 * b + k], device_id=to, device_id_type=MESH)

        me, sibling = (x, y, c), (x, y, 1 - c)
        first = [copy(a, j, 2 * x + y, c, (*chips[j], c), True) for a in range(n) for j in range(3)]
        first += [whole_copy(b, j, 2 * x + y, (*chips[j], c)) for b in range(nw) for j in range(3)]
        landing = [copy(a, j, chip_ids[j], c, me, False) for a in range(n) for j in range(3)]
        passed = [copy(a, 3 + j, chip_ids[j], c, sibling, False) for a in range(n) for j in range(3)]
        arriving = [copy(a, 3 + j, chip_ids[j], 1 - c, me, False) for a in range(n) for j in range(3)]
        arriving += [whole_copy(b, j, chip_ids[j], me) for b in range(nw) for j in range(3)]
        return first, landing, passed, arriving

    def start(*refs):
        for cp in plan(*refs)[0]:
            cp.start()

    def finish(*refs):
        first, landing, passed, arriving = plan(*refs)
        for lands, onward in zip(landing, passed):
            lands.wait_recv()
            onward.start()
        for cp in arriving:
            cp.wait_recv()
        for cp in first + passed:
            cp.wait_send()

    out_shape = [jax.ShapeDtypeStruct((NCHIP,) + s.shape, s.dtype) for s in list(shards) + list(whole)]
    return _Exchange(list(shards) + list(whole), out_shape, 6 * n + 3 * nw, start, finish)


def _with_own(gathered, own):
    x, y, _ = _place()
    return lax.dynamic_update_index_in_dim(gathered, own, 2 * x + y, axis=0)


def _simple_exchange(inputs, out_shape, copies_of):
    def start(*refs):
        for cp in copies_of(*refs):
            cp.start()

    def finish(*refs):
        for cp in copies_of(*refs):
            cp.wait()

    return _Exchange(list(inputs), out_shape, len(out_shape) * 3, start, finish)


def _pair_exchange(grads):
    def copies_of(src, outs, send_sems, recv_sems):
        x, y, c = _place()
        return [pltpu.make_async_remote_copy(
            src_ref=src[a].at[:, _half_rows(src[a].shape[1], 1 - c)], dst_ref=outs[a], send_sem=send_sems.at[a],
            recv_sem=recv_sems.at[a], device_id=(x, y, 1 - c), device_id_type=MESH) for a in range(len(src))]

    return _simple_exchange(
        grads, [jax.ShapeDtypeStruct((g.shape[0], g.shape[1] // 2, g.shape[2]), g.dtype) for g in grads], copies_of)


def _pair_sum(grads, theirs, name):
    n = len(grads)

    def body(*refs):
        south = lax.axis_index("c") == 0
        for a in range(n):
            g = refs[a][...]
            half = g.shape[0] // 2
            mine = jnp.where(south, g[:half], g[half:])
            refs[2 * n + a][...] = (mine.astype(F32) + refs[n + a][...].astype(F32)).astype(BF16)

    def specs(arrs):
        return [pl.BlockSpec((None,) + g.shape[1:], lambda j: (j, 0, 0)) for g in arrs]

    return pl.pallas_call(
        body, name=name, grid=(NCHIP,), in_specs=specs(grads) + specs(theirs), out_specs=specs(theirs),
        out_shape=[jax.ShapeDtypeStruct(g.shape, BF16) for g in theirs], compiler_params=_cparams(),
    )(*grads, *theirs)


def _chip_exchange(parts):
    def copies_of(src, outs, send_sems, recv_sems):
        x, y, c = _place()
        return [pltpu.make_async_remote_copy(
            src_ref=src[a].at[2 * chip[0] + chip[1]], dst_ref=outs[a].at[k], send_sem=send_sems.at[3 * a + k],
            recv_sem=recv_sems.at[3 * a + k], device_id=(*chip, c), device_id_type=MESH)
            for a in range(len(src)) for k, chip in enumerate(_other_chips(x, y))]

    return _simple_exchange(parts, [jax.ShapeDtypeStruct((NCHIP - 1,) + p.shape[1:], p.dtype) for p in parts], copies_of)


def _chip_sum(parts, received):
    n = len(parts)
    steps = 4

    def body(*refs):
        chip = 2 * lax.axis_index("x") + lax.axis_index("y")
        for a in range(n):
            p, r = refs[a], refs[n + a]
            own = jnp.where(chip == 0, p[0], jnp.where(chip == 1, p[1], jnp.where(chip == 2, p[2], p[3])))
            refs[2 * n + a][...] = ((own.astype(F32) + r[0].astype(F32)) + r[1].astype(F32)) + r[2].astype(F32)

    def specs(arrs):
        return [pl.BlockSpec((g.shape[0], g.shape[1] // steps, g.shape[2]), lambda i: (0, i, 0)) for g in arrs]

    out_specs = [pl.BlockSpec((g.shape[1] // steps, g.shape[2]), lambda i: (i, 0)) for g in parts]
    return pl.pallas_call(
        body, name="grads_chip_sum", grid=(steps,), in_specs=specs(parts) + specs(received), out_specs=out_specs,
        out_shape=[jax.ShapeDtypeStruct(g.shape[1:], F32) for g in parts], compiler_params=_cparams(),
    )(*parts, *received)


def _pair_share(halves):
    def copies_of(src, outs, send_sems, recv_sems):
        x, y, c = _place()
        return [pltpu.make_async_remote_copy(
            src_ref=src[a], dst_ref=outs[a], send_sem=send_sems.at[a], recv_sem=recv_sems.at[a],
            device_id=(x, y, 1 - c), device_id_type=MESH) for a in range(len(src))]

    return _simple_exchange(halves, [jax.ShapeDtypeStruct(h.shape, F32) for h in halves], copies_of)


def _adamw_math(w, g, m, v):
    nm = ADAM_B1 * m + (1.0 - ADAM_B1) * g
    nv = ADAM_B2 * v + (1.0 - ADAM_B2) * jnp.square(g)
    m_hat = nm / (1.0 - ADAM_B1 ** ADAM_STEP)
    v_hat = nv / (1.0 - ADAM_B2 ** ADAM_STEP)
    return -ADAM_LR * (m_hat / (jnp.sqrt(v_hat) + ADAM_EPS) + ADAM_WD * w), nm, nv


def _adamw_big(ws, g_mine, g_theirs, ms, vs):
    n = len(ws)
    steps = 8

    def body(*refs):
        own_half = (pl.program_id(0) // (steps // 2)) == lax.axis_index("c")
        for a in range(n):
            w = refs[a][...]
            g = jnp.where(own_half, refs[n + a][...], refs[2 * n + a][...])[:, :w.shape[1]]
            d, nm, nv = _adamw_math(w, g, refs[3 * n + a][...], refs[4 * n + a][...])
            refs[5 * n + a][...] = g
            refs[6 * n + a][...] = d
            refs[7 * n + a][...] = nm
            refs[8 * n + a][...] = nv

    specs = [pl.BlockSpec((w.shape[0] // steps, w.shape[1]), lambda i: (i, 0)) for w in ws]
    half_specs = [pl.BlockSpec((g.shape[0] // (steps // 2), g.shape[1]), lambda i: (i % (steps // 2), 0)) for g in g_mine]
    shapes = [jax.ShapeDtypeStruct(w.shape, F32) for w in ws]
    res = pl.pallas_call(
        body, name="adamw_big", grid=(steps,), in_specs=specs + half_specs * 2 + specs * 2, out_specs=specs * 4,
        out_shape=shapes * 4, compiler_params=_cparams(),
    )(*ws, *g_mine, *g_theirs, *ms, *vs)
    return res[:n], res[n:2 * n], res[2 * n:3 * n], res[3 * n:]


NORM_NAMES = ("pre_mix_norm", "post_mix_norm", "pre_mlp_norm", "post_mlp_norm")
SMALL_NAMES = NORM_NAMES + ("gdn_conv_w", "fox_f_bias", "gdn_dt_bias", "gdn_a_log", "fox_out_norm", "gdn_out_norm")
CONV_COLS = 3 * DGDN // NCHIP


def _small_allreduce(d_norms, d_conv, sums, d_fox_norm, d_gdn_norm):
    def body(*refs):
        dn_ref, dconv_ref, sums_ref, dfn_ref, dgn_ref = refs[:5]
        outs = refs[5:10]
        g_norms, g_conv, g_sums, g_fn, g_gn, send_sems, recv_sems, local_sem = refs[10:]
        x, y, c = _place()
        me = 4 * x + 2 * y + c
        g_norms[me] = dn_ref[...]
        g_sums[me] = sums_ref[...]
        g_fn[me] = dfn_ref[...]
        g_gn[me] = dgn_ref[...]

        def conv_cols(chip_index):
            return dconv_ref.at[:, pl.ds(pl.multiple_of(chip_index * CONV_COLS, LANES), CONV_COLS)]

        own = pltpu.make_async_copy(conv_cols(2 * x + y), g_conv.at[me], local_sem)
        own.start()
        copies = []
        for k in range(1, NDEV):
            px, py, pc = x ^ ((k >> 2) & 1), y ^ ((k >> 1) & 1), c ^ (k & 1)
            pairs = [(dn_ref, g_norms), (conv_cols(2 * px + py), g_conv), (sums_ref, g_sums), (dfn_ref, g_fn), (dgn_ref, g_gn)]
            for a, (src, dst) in enumerate(pairs):
                cp = pltpu.make_async_remote_copy(
                    src_ref=src, dst_ref=dst.at[me], send_sem=send_sems.at[5 * (k - 1) + a],
                    recv_sem=recv_sems.at[5 * (k - 1) + a], device_id=(px, py, pc), device_id_type=MESH)
                cp.start()
                copies.append(cp)
        own.wait()
        for cp in copies:
            cp.wait()

        def total(buf):
            acc = buf[0]
            for i in range(1, NDEV):
                acc = acc + buf[i]
            return acc

        for out, buf in zip(outs, (g_norms, g_conv, g_sums, g_fn, g_gn)):
            out[...] = total(buf)

    n_sem = 5 * (NDEV - 1)
    shapes = [(4, D), (CONV_K, CONV_COLS), (8, LANES), (1, LANES), (1, LANES)]
    return pl.pallas_call(
        body, name="small_allreduce", in_specs=[VMEM] * 5, out_specs=[VMEM] * 5,
        out_shape=[jax.ShapeDtypeStruct(s, F32) for s in shapes],
        scratch_shapes=[pltpu.VMEM((NDEV,) + s, F32) for s in shapes]
        + [pltpu.SemaphoreType.DMA((n_sem,)), pltpu.SemaphoreType.DMA((n_sem,)), pltpu.SemaphoreType.DMA],
        compiler_params=_cparams(),
    )(d_norms, d_conv, sums, d_fox_norm, d_gdn_norm)


def _small_adamw(totals, ws, ms, vs):
    n = len(SMALL_NAMES)

    def body(*refs):
        t_norms, t_conv, t_sums, t_fn, t_gn = [r[...] for r in refs[:5]]
        w_refs, m_refs, v_refs = refs[5:5 + n], refs[5 + n:5 + 2 * n], refs[5 + 2 * n:5 + 3 * n]
        outs = refs[5 + 3 * n:]
        grads = [t_norms[i:i + 1, :] for i in range(4)] + [
            t_conv, t_sums[0:1, 0:NFH], t_sums[1:2, 0:NGH], t_sums[2:3, 0:NGH], t_fn[:, 0:FHD], t_gn]
        for a in range(n):
            d, nm, nv = _adamw_math(w_refs[a][...], grads[a], m_refs[a][...], v_refs[a][...])
            outs[a][...] = grads[a]
            outs[n + a][...] = d
            outs[2 * n + a][...] = nm
            outs[3 * n + a][...] = nv

    def whole(arr):
        return pl.BlockSpec(arr.shape, lambda i: (0, 0))

    res = pl.pallas_call(
        body, name="small_adamw", grid=(1,), in_specs=[whole(t) for t in totals] + [whole(w) for w in ws] * 3,
        out_specs=[whole(w) for w in ws] * 4, out_shape=[jax.ShapeDtypeStruct(w.shape, F32) for w in ws] * 4,
        compiler_params=_cparams(),
    )(*totals, *ws, *ms, *vs)
    return res[:n], res[n:2 * n], res[2 * n:3 * n], res[3 * n:]


def _to_padded_cols(w):
    pieces = [w[:, part * DFOX + hp * LANES:part * DFOX + (hp + 1) * LANES] for hp in range(NPAIR) for part in range(3)]
    pieces += [w[:, 1544:3080], w[:, 3088:3600], w[:, 1536:1544], w[:, 3080:3088],
               jnp.zeros((w.shape[0], 2 * LANES - 16), w.dtype)]
    return jnp.concatenate(pieces, axis=1)


def _from_padded_cols(w):
    c0 = BLK_SMALL * LANES
    fox = [w[:, (3 * hp + part) * LANES:(3 * hp + part + 1) * LANES] for part in range(3) for hp in range(NPAIR)]
    return jnp.concatenate(fox + [w[:, c0:c0 + 8], w[:, BLK_GDN * LANES:BLK_GZ * LANES], w[:, c0 + 8:c0 + 16],
                                  w[:, BLK_GZ * LANES:BLK_SMALL * LANES]], axis=1)


def _local_step(x, target, win_p, late_weights, reduce_late, reduce_in, pre_mix_norm, fox_f_bias, fox_out_norm, conv_w,
                gdn_a_log, gdn_dt_bias, gdn_out_norm, post_mix_norm, pre_mlp_norm, post_mlp_norm):
    bias_vec = jnp.zeros((1, LANES), F32).at[0, 0:NFH].set(fox_f_bias).at[0, LANE_G:LANE_G + NGH].set(gdn_dt_bias)
    alog_vec = jnp.zeros((1, LANES), F32).at[0, LANE_G:LANE_G + NGH].set(gdn_a_log)
    w2 = jnp.concatenate([fox_out_norm, fox_out_norm], axis=1)

    h = _pre_norm(x, pre_mix_norm)
    proj = _matmul(h, win_p, tm=2048, tn=768, tk=1024, name="mm_proj")
    gates = _gates(proj, bias_vec, alog_vec)
    mix, fox_o, lse, late_a = _fox_fwd(proj, gates, w2, exchange=late_weights[0])
    qkv = _gdn_pre(proj, conv_w)
    (u, w, qd, kd, a_intra, gl, t_inv), late_b = _gdn_prep(qkv, gates, exchange=late_weights[1])
    wout, wup3, wdown = late_weights[2](late_a, late_b)
    mix, gdn_raw, states = _gdn_scan(u, w, qd, kd, a_intra, gl, proj, gdn_out_norm, mix)
    mixed = _matmul(mix, wout, tm=2048, tk=1024, name="mm_out")
    x1, h2 = _post_mix(x, mixed, post_mix_norm, pre_mlp_norm)

    def relu2(acc):
        r = jnp.maximum(acc, 0.0)
        return acc, r * r

    up, act = _matmul(h2, wup3, b3=True, tm=1024, tn=1024, tk=1024, out_dtypes=(F32, BF16), epilogue=relu2, name="mm_up")
    y = _matmul(act, wdown, tm=2048, tk=1024, name="mm_down")
    dx2, dy, d_post_mlp, loss_row = _loss_head(x1, y, post_mlp_norm, target)

    dwdown = _matmul(act, dy, ta=True, tm=1024, tn=1024, tk=2048, out_dtypes=(BF16,), name="mm_dwdown")

    def relu2_bwd(acc, upv):
        return (acc * 2.0 * jnp.maximum(upv, 0.0),)

    dup = _matmul(dy, wdown, tb=True, tm=1024, tn=1024, tk=1024, out_dtypes=(BF16,), extra=(up,), epilogue=relu2_bwd,
                  name="mm_dact")
    dwup3 = _matmul(h2, dup, ta=True, tm=1024, tn=1024, tk=2048, out_dtypes=(BF16,), o3=True, name="mm_dwup")
    dh2 = _matmul(dup, wup3, tb=True, b3=True, tm=2048, tk=1024, name="mm_dh2")
    dx1, dmixed, d_pre_mlp, d_post_mix = _mid_bwd(dh2, x1, pre_mlp_norm, dx2, mixed, post_mix_norm)
    dwout = _matmul(mix, dmixed, ta=True, tm=1024, tn=1024, tk=2048, out_dtypes=(BF16,), name="mm_dwout")
    dmix = _matmul(dmixed, wout, tb=True, tm=2048, tk=1024, name="mm_dmix")

    dfox, delta, d_fox_norm = _fox_norm_bwd(fox_o, dmix, w2)
    dproj, dcum_fox, reduced_late = _fox_bwd(proj, dfox, gates, lse, delta, exchange=reduce_late(dwout, dwup3, dwdown))
    dproj, du, dw, dqd, dkd, da, dgl, d_gdn_norm = _gdn_scan_bwd(dmix, gdn_raw, proj, gdn_out_norm, u, w, qd, kd,
                                                                 a_intra, gl, states, dproj)
    dqkv, dgates_gdn = _gdn_prep_bwd(qkv, gates, t_inv, du, dw, dqd, dkd, da, dgl)
    dproj, d_conv = _gdn_pre_bwd(proj, conv_w, dqkv, dproj)
    dproj, sums = _gates_bwd(proj, bias_vec, alog_vec, dgates_gdn, dcum_fox, dproj)

    dwin_p = _matmul(h, dproj, ta=True, tm=1024, tn=1280, tk=2048, out_dtypes=(BF16,), name="mm_dwin")
    exchange_in = reduce_in(dwin_p)
    dh = _matmul(dproj, win_p, tb=True, tm=2048, tk=1280, name="mm_dh", exchange=exchange_in)
    dh, reduced_in = dh if exchange_in is not None else (dh, [])
    grad_x, d_pre_mix = _pre_norm_bwd(dh, x, pre_mix_norm, dx1)

    d_norms = jnp.concatenate([d_pre_mix, d_post_mix, d_pre_mlp, d_post_mlp], axis=0)
    return loss_row[0, 0], grad_x, (d_norms, d_conv, sums, d_fox_norm, d_gdn_norm), reduced_late, reduced_in


def kernel(x, pre_mix_norm, w_in, fox_f_bias, fox_out_norm, gdn_conv_w, gdn_a_log, gdn_dt_bias, gdn_out_norm, w_out, post_mix_norm, pre_mlp_norm, w_up, w_down, post_mlp_norm, loss_target, m_pre_mix_norm, m_w_in, m_fox_f_bias, m_fox_out_norm, m_gdn_conv_w, m_gdn_a_log, m_gdn_dt_bias, m_gdn_out_norm, m_w_out, m_post_mix_norm, m_pre_mlp_norm, m_w_up, m_w_down, m_post_mlp_norm, v_pre_mix_norm, v_w_in, v_fox_f_bias, v_fox_out_norm, v_gdn_conv_w, v_gdn_a_log, v_gdn_dt_bias, v_gdn_out_norm, v_w_out, v_post_mix_norm, v_pre_mlp_norm, v_w_up, v_w_down, v_post_mlp_norm):
    weights = dict(pre_mix_norm=pre_mix_norm, w_in=w_in, fox_f_bias=fox_f_bias, fox_out_norm=fox_out_norm, gdn_conv_w=gdn_conv_w,
                   gdn_a_log=gdn_a_log, gdn_dt_bias=gdn_dt_bias, gdn_out_norm=gdn_out_norm, w_out=w_out, post_mix_norm=post_mix_norm,
                   pre_mlp_norm=pre_mlp_norm, w_up=w_up, w_down=w_down, post_mlp_norm=post_mlp_norm)
    m_in = dict(pre_mix_norm=m_pre_mix_norm, w_in=m_w_in, fox_f_bias=m_fox_f_bias, fox_out_norm=m_fox_out_norm, gdn_conv_w=m_gdn_conv_w,
                gdn_a_log=m_gdn_a_log, gdn_dt_bias=m_gdn_dt_bias, gdn_out_norm=m_gdn_out_norm, w_out=m_w_out, post_mix_norm=m_post_mix_norm,
                pre_mlp_norm=m_pre_mlp_norm, w_up=m_w_up, w_down=m_w_down, post_mlp_norm=m_post_mlp_norm)
    v_in = dict(pre_mix_norm=v_pre_mix_norm, w_in=v_w_in, fox_f_bias=v_fox_f_bias, fox_out_norm=v_fox_out_norm, gdn_conv_w=v_gdn_conv_w,
                gdn_a_log=v_gdn_a_log, gdn_dt_bias=v_gdn_dt_bias, gdn_out_norm=v_gdn_out_norm, w_out=v_w_out, post_mix_norm=v_post_mix_norm,
                pre_mlp_norm=v_pre_mlp_norm, w_up=v_w_up, w_down=v_w_down, post_mlp_norm=v_post_mlp_norm)
    order_w = ("pre_mix_norm", "w_in", "fox_f_bias", "fox_out_norm", "gdn_conv_w", "gdn_a_log", "gdn_dt_bias", "gdn_out_norm", "w_out",
               "post_mix_norm", "pre_mlp_norm", "w_up", "w_down", "post_mlp_norm")
    big = ("w_in", "w_out", "w_up", "w_down")

    def row(v):
        return v if v.ndim == 2 else v.reshape(1, -1)

    cw = DPROJ // NCHIP
    win_shard = jnp.pad(w_in.astype(BF16), ((0, 0), (0, D - cw)))
    win_g, conv_g = _run_exchange(_allgather_exchange([win_shard], whole=[gdn_conv_w]), "weights_allgather_in")
    win_p = _to_padded_cols(_with_own(win_g, win_shard)[:, :, :cw].transpose(1, 0, 2).reshape(D, DPROJ))
    conv_full = _with_own(conv_g, gdn_conv_w).transpose(1, 0, 2).reshape(CONV_K, 3 * DGDN)
    late_shards = [weights[n].astype(BF16) for n in big[1:]]

    def resolve_late(gathered_a, gathered_b):
        wout_g, wup3, wdown_g = [_with_own(g, own) for g, own in zip(list(gathered_a) + list(gathered_b), late_shards)]
        return wout_g.reshape(D, D), wup3, wdown_g.reshape(DFF, D)

    pair_sums = {}

    def pair_reduced(names, blocks):
        theirs = _run_exchange(_pair_exchange(blocks), "grads_pair_exchange_" + names[0])
        for n, s in zip(names, _pair_sum(blocks, theirs, "grads_pair_sum_" + names[0])):
            pair_sums[n] = s
        return _chip_exchange([pair_sums[n] for n in names])

    def reduce_late(dwout, dwup3, dwdown):
        return pair_reduced(big[1:], [dwout.reshape(NCHIP, D // NCHIP, D), dwup3, dwdown.reshape(NCHIP, DFF // NCHIP, D)])

    def reduce_in(dwin_p):
        dwin3 = _from_padded_cols(dwin_p).reshape(D, NCHIP, cw).transpose(1, 0, 2)
        return pair_reduced(big[:1], [jnp.pad(dwin3, ((0, 0), (0, 0), (0, D - cw)))])

    loss_local, grad_x, small, received_late, received_in = _local_step(
        x[0], loss_target[0], win_p, (_allgather_exchange(late_shards[:2]), _allgather_exchange(late_shards[2:]), resolve_late),
        reduce_late, reduce_in, row(pre_mix_norm), fox_f_bias, row(fox_out_norm), conv_full, gdn_a_log, gdn_dt_bias,
        row(gdn_out_norm), row(post_mix_norm), row(pre_mlp_norm), row(post_mlp_norm))
    loss = lax.psum(loss_local, ("x", "y", "c"))

    g_mine = _chip_sum([pair_sums[n] for n in big], list(received_in) + list(received_late))
    g_theirs = _run_exchange(_pair_share(g_mine), "grads_pair_share")

    g_small, d_small, nm_small, nv_small = _small_adamw(
        _small_allreduce(*small), [row(weights[n]) for n in SMALL_NAMES], [row(m_in[n]) for n in SMALL_NAMES],
        [row(v_in[n]) for n in SMALL_NAMES])

    g_big, d_big, nm_big, nv_big = _adamw_big([weights[n] for n in big], g_mine, g_theirs, [m_in[n] for n in big],
                                              [v_in[n] for n in big])

    grads, delta, new_m, new_v = {}, {}, {}, {}
    for i, n in enumerate(big):
        grads[n], delta[n], new_m[n], new_v[n] = g_big[i], d_big[i], nm_big[i], nv_big[i]
    for i, n in enumerate(SMALL_NAMES):
        shape = weights[n].shape
        grads[n], delta[n], new_m[n], new_v[n] = (g_small[i].reshape(shape), d_small[i].reshape(shape),
                                                  nm_small[i].reshape(shape), nv_small[i].reshape(shape))
    return (loss, grad_x[None], *[grads[n] for n in order_w], *[delta[n] for n in order_w], *[new_m[n] for n in order_w],
            *[new_v[n] for n in order_w])
```

```python
import jax
import jax.numpy as jnp
from jax import lax
from jax.experimental import pallas as pl
from jax.experimental.pallas import tpu as pltpu

F32 = jnp.float32
BF16 = jnp.bfloat16
MESH = pl.DeviceIdType.MESH

S = 2048
D = 1024
NFH, FHD = 8, 64
NPAIR = NFH // 2
NGH, GHD = 4, 128
DFOX = NFH * FHD
DGDN = NGH * GHD
CHUNK = 64
NCH = S // CHUNK
CONV_K = 4
DFF = 4 * D
EPS = 1e-6
DPROJ = 3600
LANES = 128
DPROJ_PAD = 3840
BLK_GDN = 12
BLK_GZ = 24
BLK_SMALL = 28
NCHIP = 4
NDEV = 8
VMEM_LIMIT = 56 * 1024 * 1024

ADAM_LR = 0.001
ADAM_B1 = 0.9
ADAM_B2 = 0.999
ADAM_EPS = 1e-08
ADAM_WD = 0.01
ADAM_STEP = 10


def _cparams(**kw):
    return pltpu.CompilerParams(vmem_limit_bytes=VMEM_LIMIT, **kw)


def _dn(ca, cb):
    return (((ca,), (cb,)), ((), ()))


def _dot(a, b, ca=1, cb=0):
    return lax.dot_general(a.astype(BF16), b.astype(BF16), _dn(ca, cb), preferred_element_type=F32)


def _hdot(a, b, ca=1, cb=0):
    return lax.dot_general(a.astype(F32), b.astype(F32), _dn(ca, cb), precision=lax.Precision.HIGHEST,
                           preferred_element_type=F32)


@jax.custom_vjp
def _mm_nn(a, b):
    return _dot(a, b, 1, 0)


def _mm_nn_fwd(a, b):
    return _dot(a, b, 1, 0), (a, b)


def _mm_nn_bwd(res, g):
    a, b = res
    return _dot(g, b, 1, 1), _dot(a, g, 0, 0)


_mm_nn.defvjp(_mm_nn_fwd, _mm_nn_bwd)


@jax.custom_vjp
def _mm_nt(a, b):
    return _dot(a, b, 1, 1)


def _mm_nt_fwd(a, b):
    return _dot(a, b, 1, 1), (a, b)


def _mm_nt_bwd(res, g):
    a, b = res
    return _dot(g, b, 1, 0), _dot(g, a, 0, 0)


_mm_nt.defvjp(_mm_nt_fwd, _mm_nt_bwd)


@jax.custom_vjp
def _saved_inverse(m, t_inv):
    del m
    return t_inv


def _saved_inverse_fwd(m, t_inv):
    del m
    return t_inv, t_inv


def _saved_inverse_bwd(t_inv, g):
    return -_hdot(_hdot(t_inv, g, 0, 0), t_inv, 1, 1), jnp.zeros_like(t_inv)


_saved_inverse.defvjp(_saved_inverse_fwd, _saved_inverse_bwd)


def _sigmoid(z):
    return 1.0 / (1.0 + jnp.exp(-z))


def _softplus(z):
    return jnp.maximum(z, 0.0) + jnp.log(1.0 + jnp.exp(-jnp.abs(z)))


def _silu(z):
    return z * _sigmoid(z)


def _rms_scale(x):
    return lax.rsqrt(jnp.mean(x * x, axis=-1, keepdims=True) + EPS)


def _rms_bwd(x, w, g):
    r = _rms_scale(x)
    gw = g * w
    dx = r * gw - x * (r * r * r) * jnp.mean(gw * x, axis=-1, keepdims=True)
    return dx, g * x * r


def _matmul(a, b, *, name, ta=False, tb=False, tm=512, tn=512, tk=512, out_dtypes=(F32,), b3=False, o3=False,
            extra=(), epilogue=None, exchange=None):
    m, k = (a.shape[1], a.shape[0]) if ta else a.shape
    if b3:
        n = b.shape[1] if tb else b.shape[0] * b.shape[2]
        kb = b.shape[0] * b.shape[2] if tb else b.shape[1]
    else:
        n, kb = (b.shape[0], b.shape[1]) if tb else (b.shape[1], b.shape[0])
    assert kb == k, (name, kb, k)
    tm, tn, tk = min(tm, m), min(tn, n), min(tk, k)
    assert m % tm == 0 and n % tn == 0 and k % tk == 0, (name, m, n, k, tm, tn, tk)
    nk = k // tk
    n_extra = len(extra)
    n_out = len(out_dtypes)
    grid = (m // tm, n // tn, nk)
    ex_in, ex_in_specs, ex_out_specs, ex_out_shape, ex_scratch = _hosted(exchange)

    def body(*refs):
        a_ref, b_ref = refs[0], refs[1]
        extra_refs = refs[2:2 + n_extra]
        first_out = 2 + n_extra + len(ex_in)
        out_refs = refs[first_out:first_out + n_out]
        ex_refs = refs[2 + n_extra:first_out] + refs[first_out + n_out:first_out + n_out + len(ex_out_shape)] + refs[-2:]
        step = [pl.program_id(d) for d in range(3)]

        if exchange is not None:
            @pl.when((step[0] == 0) & (step[1] == 0) & (step[2] == 0))
            def _():
                exchange.start(*exchange.split(ex_refs))

        def finish(acc):
            outs = (acc,) if epilogue is None else epilogue(acc, *[r[...] for r in extra_refs])
            for o_ref, val in zip(out_refs, outs):
                o_ref[...] = val.astype(o_ref.dtype)

        part = _dot(a_ref[...], b_ref[...], 0 if ta else 1, 1 if tb else 0)
        if nk == 1:
            finish(part)
        else:
            acc_ref = refs[first_out + n_out + len(ex_out_shape)]

            @pl.when(step[2] == 0)
            def _():
                acc_ref[...] = part

            @pl.when(step[2] > 0)
            def _():
                acc_ref[...] += part

            @pl.when(step[2] == nk - 1)
            def _():
                finish(acc_ref[...])

        if exchange is not None:
            @pl.when((step[0] == grid[0] - 1) & (step[1] == grid[1] - 1) & (step[2] == nk - 1))
            def _():
                exchange.finish(*exchange.split(ex_refs))

    a_spec = pl.BlockSpec((tk, tm), lambda i, j, kk: (kk, i)) if ta else pl.BlockSpec((tm, tk), lambda i, j, kk: (i, kk))
    if b3 and tb:
        assert b.shape[2] == tk
        b_spec = pl.BlockSpec((None, tn, tk), lambda i, j, kk: (kk, j, 0))
    elif b3:
        assert b.shape[2] == tn
        b_spec = pl.BlockSpec((None, tk, tn), lambda i, j, kk: (j, kk, 0))
    elif tb:
        b_spec = pl.BlockSpec((tn, tk), lambda i, j, kk: (j, kk))
    else:
        b_spec = pl.BlockSpec((tk, tn), lambda i, j, kk: (kk, j))
    tile = pl.BlockSpec((tm, tn), lambda i, j, kk: (i, j))
    out_specs = [tile] * n_out
    out_shape = [jax.ShapeDtypeStruct((m, n), dt) for dt in out_dtypes]
    if o3:
        out_specs[0] = pl.BlockSpec((None, tm, tn), lambda i, j, kk: (j, i, 0))
        out_shape[0] = jax.ShapeDtypeStruct((n // tn, m, tn), out_dtypes[0])
    res = pl.pallas_call(
        body, name=name, grid=grid,
        in_specs=[a_spec, b_spec] + [tile] * n_extra + ex_in_specs, out_specs=out_specs + ex_out_specs,
        out_shape=out_shape + ex_out_shape,
        scratch_shapes=([pltpu.VMEM((tm, tn), F32)] if nk > 1 else []) + ex_scratch,
        compiler_params=_cparams(),
    )(a, b, *extra, *ex_in)
    if exchange is not None:
        return (res[0] if n_out == 1 else res[:n_out]), res[n_out:]
    return res[0] if n_out == 1 else res


TR = 256


def _row_spec(cols):
    return pl.BlockSpec((TR, cols), lambda i: (i, 0))


def _vec_spec(cols):
    return pl.BlockSpec((1, cols), lambda i: (0, 0))


def _pre_norm(x, w):
    def body(x_ref, w_ref, h_ref):
        xv = x_ref[...]
        h_ref[...] = (xv * _rms_scale(xv) * w_ref[...]).astype(BF16)

    return pl.pallas_call(
        body, name="pre_norm", grid=(S // TR,), in_specs=[_row_spec(D), _vec_spec(D)], out_specs=_row_spec(D),
        out_shape=jax.ShapeDtypeStruct((S, D), BF16), compiler_params=_cparams(),
    )(x, w)


def _post_mix(x, mixed, w_post, w_pre_mlp):
    def body(x_ref, m_ref, wp_ref, wm_ref, x1_ref, h2_ref):
        mv = m_ref[...]
        x1 = x_ref[...] + mv * _rms_scale(mv) * wp_ref[...]
        x1_ref[...] = x1
        h2_ref[...] = (x1 * _rms_scale(x1) * wm_ref[...]).astype(BF16)

    return pl.pallas_call(
        body, name="post_mix", grid=(S // TR,),
        in_specs=[_row_spec(D), _row_spec(D), _vec_spec(D), _vec_spec(D)], out_specs=[_row_spec(D), _row_spec(D)],
        out_shape=[jax.ShapeDtypeStruct((S, D), F32), jax.ShapeDtypeStruct((S, D), BF16)], compiler_params=_cparams(),
    )(x, mixed, w_post, w_pre_mlp)


def _loss_head(x1, y, w_post_mlp, target):
    def body(x1_ref, y_ref, w_ref, t_ref, dx2_ref, dy_ref, dw_ref, loss_ref):
        i = pl.program_id(0)
        yv = y_ref[...]
        w = w_ref[...]
        x2 = x1_ref[...] + yv * _rms_scale(yv) * w
        err = x2 - t_ref[...]
        dx2 = err * (1.0 / D)
        dx2_ref[...] = dx2
        dy, dwt = _rms_bwd(yv, w, dx2)
        dy_ref[...] = dy.astype(BF16)

        @pl.when(i == 0)
        def _():
            dw_ref[...] = jnp.zeros_like(dw_ref)
            loss_ref[...] = jnp.zeros_like(loss_ref)

        dw_ref[...] += jnp.sum(dwt, axis=0, keepdims=True)
        part = 0.5 * jnp.sum(jnp.mean(err * err, axis=-1, keepdims=True), axis=0, keepdims=True)
        loss_ref[...] += jnp.broadcast_to(part, loss_ref.shape)

    return pl.pallas_call(
        body, name="loss_head", grid=(S // TR,),
        in_specs=[_row_spec(D), _row_spec(D), _vec_spec(D), _row_spec(D)],
        out_specs=[_row_spec(D), _row_spec(D), _vec_spec(D), _vec_spec(LANES)],
        out_shape=[jax.ShapeDtypeStruct((S, D), F32), jax.ShapeDtypeStruct((S, D), BF16),
                   jax.ShapeDtypeStruct((1, D), F32), jax.ShapeDtypeStruct((1, LANES), F32)],
        compiler_params=_cparams(),
    )(x1, y, w_post_mlp, target)


def _mid_bwd(dh2, x1, w_pre_mlp, dx2, mixed, w_post):
    def body(dh2_ref, x1_ref, wm_ref, dx2_ref, m_ref, wp_ref, dx1_ref, dm_ref, dwm_ref, dwp_ref):
        i = pl.program_id(0)
        dxa, dwm = _rms_bwd(x1_ref[...], wm_ref[...], dh2_ref[...])
        dx1 = dx2_ref[...] + dxa
        dx1_ref[...] = dx1
        dm, dwp = _rms_bwd(m_ref[...], wp_ref[...], dx1)
        dm_ref[...] = dm.astype(BF16)

        @pl.when(i == 0)
        def _():
            dwm_ref[...] = jnp.zeros_like(dwm_ref)
            dwp_ref[...] = jnp.zeros_like(dwp_ref)

        dwm_ref[...] += jnp.sum(dwm, axis=0, keepdims=True)
        dwp_ref[...] += jnp.sum(dwp, axis=0, keepdims=True)

    return pl.pallas_call(
        body, name="mid_bwd", grid=(S // TR,),
        in_specs=[_row_spec(D), _row_spec(D), _vec_spec(D), _row_spec(D), _row_spec(D), _vec_spec(D)],
        out_specs=[_row_spec(D), _row_spec(D), _vec_spec(D), _vec_spec(D)],
        out_shape=[jax.ShapeDtypeStruct((S, D), F32), jax.ShapeDtypeStruct((S, D), BF16),
                   jax.ShapeDtypeStruct((1, D), F32), jax.ShapeDtypeStruct((1, D), F32)],
        compiler_params=_cparams(),
    )(dh2, x1, w_pre_mlp, dx2, mixed, w_post)


def _pre_norm_bwd(dh, x, w, dx1):
    def body(dh_ref, x_ref, w_ref, dx1_ref, dx_ref, dw_ref):
        i = pl.program_id(0)
        dxa, dwt = _rms_bwd(x_ref[...], w_ref[...], dh_ref[...])
        dx_ref[...] = dx1_ref[...] + dxa

        @pl.when(i == 0)
        def _():
            dw_ref[...] = jnp.zeros_like(dw_ref)

        dw_ref[...] += jnp.sum(dwt, axis=0, keepdims=True)

    return pl.pallas_call(
        body, name="pre_norm_bwd", grid=(S // TR,),
        in_specs=[_row_spec(D), _row_spec(D), _vec_spec(D), _row_spec(D)], out_specs=[_row_spec(D), _vec_spec(D)],
        out_shape=[jax.ShapeDtypeStruct((S, D), F32), jax.ShapeDtypeStruct((1, D), F32)], compiler_params=_cparams(),
    )(dh, x, w, dx1)


BQ = 256
NQ = S // BQ
LANE_BETA, LANE_G = 8, 12


def _gate_lanes(shape):
    lane = lax.broadcasted_iota(jnp.int32, shape, 1)
    return lane < LANE_BETA, (lane >= LANE_BETA) & (lane < LANE_G), (lane >= LANE_G) & (lane < LANE_G + NGH)


def _gates(proj, bias_vec, alog_vec):
    def body(s_ref, b_ref, a_ref, o_ref, carry_ref):
        i = pl.program_id(0)

        @pl.when(i == 0)
        def _():
            carry_ref[...] = jnp.zeros_like(carry_ref)

        z = s_ref[...] + b_ref[...]
        tail = jnp.log(1.0 + jnp.exp(-jnp.abs(z)))
        sp = jnp.maximum(z, 0.0) + tail
        lf = jnp.minimum(z, 0.0) - tail
        r = lax.broadcasted_iota(jnp.int32, (BQ, BQ), 0)
        c = lax.broadcasted_iota(jnp.int32, (BQ, BQ), 1)
        tri = (c <= r).astype(F32)
        cum = _hdot(tri, lf) + carry_ref[...]
        carry_ref[...] = cum[BQ - 1:BQ, :]
        is_fox, is_beta, is_g = _gate_lanes(z.shape)
        o_ref[...] = jnp.where(is_fox, cum, jnp.where(is_beta, _sigmoid(z), jnp.where(is_g, -jnp.exp(a_ref[...]) * sp, 0.0)))

    return pl.pallas_call(
        body, name="gates", grid=(NQ,),
        in_specs=[pl.BlockSpec((BQ, LANES), lambda i: (i, BLK_SMALL)), _vec_spec(LANES), _vec_spec(LANES)],
        out_specs=pl.BlockSpec((BQ, LANES), lambda i: (i, 0)), out_shape=jax.ShapeDtypeStruct((S, LANES), F32),
        scratch_shapes=[pltpu.VMEM((1, LANES), F32)], compiler_params=_cparams(),
    )(proj, bias_vec, alog_vec)


def _gates_bwd(proj, bias_vec, alog_vec, dgates_gdn, dcum_fox, dproj):
    def body(s_ref, b_ref, a_ref, dg_ref, dc_ref, dproj_in, dproj_ref, red_ref, carry_ref):
        del dproj_in
        i = pl.program_id(0)

        @pl.when(i == 0)
        def _():
            carry_ref[...] = jnp.zeros_like(carry_ref)
            red_ref[...] = jnp.zeros_like(red_ref)

        z = s_ref[...] + b_ref[...]
        dg = dg_ref[...] + dc_ref[...]
        r = lax.broadcasted_iota(jnp.int32, (BQ, BQ), 0)
        c = lax.broadcasted_iota(jnp.int32, (BQ, BQ), 1)
        upper = (c >= r).astype(F32)
        dlf = _hdot(upper, dg) + carry_ref[...]
        carry_ref[...] = dlf[0:1, :]
        sig = _sigmoid(z)
        g_scale = -jnp.exp(a_ref[...])
        is_fox, is_beta, is_g = _gate_lanes(z.shape)
        ds = jnp.where(is_fox, dlf * (1.0 - sig), jnp.where(is_beta, dg * sig * (1.0 - sig), jnp.where(is_g, dg * g_scale * sig, 0.0)))
        dproj_ref[:, 0:LANES] = ds.astype(BF16)
        dproj_ref[:, LANES:2 * LANES] = jnp.zeros((BQ, LANES), BF16)
        dalog = jnp.where(is_g, dg * g_scale * _softplus(z), 0.0)
        sums = jnp.sum(ds, axis=0, keepdims=True)
        red_ref[0:1, :] += jnp.where(is_fox[0:1], sums, 0.0)
        red_ref[1:2, :] += pltpu.roll(jnp.where(is_g[0:1], sums, 0.0), LANES - LANE_G, 1)
        red_ref[2:3, :] += pltpu.roll(jnp.sum(dalog, axis=0, keepdims=True), LANES - LANE_G, 1)

    blk = pl.BlockSpec((BQ, LANES), lambda i: (NQ - 1 - i, 0))
    return pl.pallas_call(
        body, name="gates_bwd", grid=(NQ,),
        in_specs=[pl.BlockSpec((BQ, LANES), lambda i: (NQ - 1 - i, BLK_SMALL)), _vec_spec(LANES), _vec_spec(LANES), blk, blk,
                  pl.BlockSpec(memory_space=pl.ANY)],
        out_specs=[pl.BlockSpec((BQ, 2 * LANES), lambda i: (NQ - 1 - i, BLK_SMALL // 2)), pl.BlockSpec((8, LANES), lambda i: (0, 0))],
        out_shape=[jax.ShapeDtypeStruct((S, DPROJ_PAD), BF16), jax.ShapeDtypeStruct((8, LANES), F32)],
        input_output_aliases={5: 0},
        scratch_shapes=[pltpu.VMEM((1, LANES), F32)], compiler_params=_cparams(),
    )(proj, bias_vec, alog_vec, dgates_gdn, dcum_fox, dproj)


FOX_SCALE = FHD ** -0.5
FOX_PAIRS = 2
FOX_PAIRS_BWD = 2


def _head_mask(e):
    lane = lax.broadcasted_iota(jnp.int32, (1, LANES), 1)
    return (lane >= e * FHD) & (lane < (e + 1) * FHD)


def _lane_col(vals, index):
    lane = lax.broadcasted_iota(jnp.int32, vals.shape, 1)
    return jnp.sum(jnp.where(lane == index, vals, 0.0), axis=1, keepdims=True)


def _sublane_row(vals, index):
    row = lax.broadcasted_iota(jnp.int32, vals.shape, 0)
    return jnp.sum(jnp.where(row == index, vals, 0.0), axis=0, keepdims=True)


def _pair_cols(c0, c1):
    lane = lax.broadcasted_iota(jnp.int32, (c0.shape[0], 2), 1)
    return jnp.where(lane == 0, c0, c1)


def _split3(x):
    hi = x.astype(BF16).astype(F32)
    rest = x - hi
    mid = rest.astype(BF16).astype(F32)
    return hi, mid, (rest - mid).astype(BF16).astype(F32)


def _fox_operand(vals, e, cum, is_query):
    lane = lax.broadcasted_iota(jnp.int32, (1, LANES), 1)
    base = (1 - e) * FHD
    parts = _split3(cum)
    own = jnp.where(_head_mask(e), vals * FOX_SCALE if is_query else vals, 0.0)
    cum_at, ones_at = (base, base + 3) if is_query else (base + 3, base)
    sign = 1.0 if is_query else -1.0
    out = own + jnp.where((lane >= ones_at) & (lane < ones_at + 3), 1.0, 0.0)
    for i, part in enumerate(parts):
        out = out + jnp.where(lane == cum_at + i, sign * part, 0.0)
    return out.astype(BF16)


def _causal_block():
    return lax.broadcasted_iota(jnp.int32, (BQ, BQ), 1) <= lax.broadcasted_iota(jnp.int32, (BQ, BQ), 0)


def _head_rms(o, masks):
    o2 = o * o
    r = [lax.rsqrt(jnp.sum(jnp.where(mk, o2, 0.0), axis=1, keepdims=True) * (1.0 / FHD) + EPS) for mk in masks]
    return jnp.where(masks[0], r[0], r[1])


def _hosted(exchange):
    if exchange is None:
        return [], [], [], [], []
    return (exchange.inputs, [HBM] * len(exchange.inputs), [HBM] * len(exchange.out_shape), exchange.out_shape,
            exchange.sem_shapes())


def _fox_fwd(proj, gates, w2, exchange=None):
    ex_in, ex_in_specs, ex_out_specs, ex_out_shape, ex_scratch = _hosted(exchange)

    n_in = 3 * FOX_PAIRS + 2
    heads = [(pp, e) for pp in range(FOX_PAIRS) for e in range(2)]

    def body(*refs):
        qkv_refs, g_ref, w_ref = refs[:3 * FOX_PAIRS], refs[3 * FOX_PAIRS], refs[3 * FOX_PAIRS + 1]
        mix_ref, o_ref, lse_ref = refs[n_in + len(ex_in):n_in + 3 + len(ex_in)]
        ka_ref, vb_ref = refs[n_in + 3 + len(ex_in) + len(ex_out_shape):n_in + 5 + len(ex_in) + len(ex_out_shape)]
        ex_refs = refs[n_in:n_in + len(ex_in)] + refs[n_in + 3 + len(ex_in):n_in + 3 + len(ex_in) + len(ex_out_shape)] + refs[-2:]
        grp, qi = pl.program_id(0), pl.program_id(1)

        def head_index(pp, e):
            return 2 * (FOX_PAIRS * grp + pp) + e

        if exchange is not None:
            @pl.when((grp == 0) & (qi == 0))
            def _():
                exchange.start(*exchange.split(ex_refs))

        @pl.when(qi == 0)
        def _():
            gt = g_ref[...]
            for pp in range(FOX_PAIRS):
                kv = qkv_refs[3 * pp + 1][...]
                for e in range(2):
                    ka_ref[2 * pp + e] = _fox_operand(kv, e, _lane_col(gt, head_index(pp, e)), False)
                vb_ref[pp] = qkv_refs[3 * pp + 2][...].astype(BF16)

        masks = [_head_mask(0), _head_mask(1)]
        gt = g_ref[pl.ds(pl.multiple_of(qi * BQ, BQ), BQ), :]
        qs = [_fox_operand(qkv_refs[3 * pp][...], e, _lane_col(gt, head_index(pp, e)), True) for pp, e in heads]
        n = range(len(heads))

        def block(kj, carry, diagonal):
            rows = pl.ds(pl.multiple_of(kj * BQ, BQ), BQ)
            s = [_dot(qs[i], ka_ref[i, rows, :], 1, 1) for i in n]
            if diagonal:
                s = [jnp.where(_causal_block(), s[i], -jnp.inf) for i in n]
            m_new = [jnp.maximum(carry[i][0], jnp.max(s[i], axis=-1, keepdims=True)) for i in n]
            p = [jnp.exp(s[i] - m_new[i]) for i in n]
            alpha = [jnp.exp(carry[i][0] - m_new[i]) for i in n]
            l_new = [alpha[i] * carry[i][1] + jnp.sum(p[i], axis=-1, keepdims=True) for i in n]
            pv = [_dot(p[i], vb_ref[heads[i][0], rows, :]) for i in n]
            return tuple((m_new[i], l_new[i], alpha[i] * carry[i][2] + pv[i]) for i in n)

        one = (jnp.full((BQ, 1), -jnp.inf, F32), jnp.zeros((BQ, 1), F32), jnp.zeros((BQ, LANES), F32))
        below = lax.fori_loop(0, qi, lambda kj, carry: block(kj, carry, False), (one,) * len(heads))
        done = block(qi, below, True)
        for pp in range(FOX_PAIRS):
            (m0, l0, a0), (m1, l1, a1) = done[2 * pp], done[2 * pp + 1]
            o = jnp.where(masks[0], a0 / l0, a1 / l1)
            cols = slice(pp * LANES, (pp + 1) * LANES)
            o_ref[:, cols] = o
            mix_ref[:, cols] = (o * _head_rms(o, masks) * w_ref[...]).astype(BF16)
            lse_ref[pp] = _pair_cols(m0 + jnp.log(l0), m1 + jnp.log(l1))

        if exchange is not None:
            @pl.when((grp == NPAIR // FOX_PAIRS - 1) & (qi == NQ - 1))
            def _():
                exchange.finish(*exchange.split(ex_refs))

    qkv_specs = []
    for pp in range(FOX_PAIRS):
        qkv_specs.append(pl.BlockSpec((BQ, LANES), lambda g, i, pp=pp: (i, 3 * (FOX_PAIRS * g + pp))))
        qkv_specs.append(pl.BlockSpec((S, LANES), lambda g, i, pp=pp: (0, 3 * (FOX_PAIRS * g + pp) + 1)))
        qkv_specs.append(pl.BlockSpec((S, LANES), lambda g, i, pp=pp: (0, 3 * (FOX_PAIRS * g + pp) + 2)))
    blk = pl.BlockSpec((BQ, FOX_PAIRS * LANES), lambda g, i: (i, g))
    res = pl.pallas_call(
        body, name="fox_fwd", grid=(NPAIR // FOX_PAIRS, NQ),
        in_specs=qkv_specs + [pl.BlockSpec((S, LANES), lambda g, i: (0, 0)), pl.BlockSpec((1, LANES), lambda g, i: (0, 0))]
        + ex_in_specs,
        out_specs=[blk, blk, pl.BlockSpec((FOX_PAIRS, BQ, 2), lambda g, i: (g, i, 0))] + ex_out_specs,
        out_shape=[jax.ShapeDtypeStruct((S, D), BF16), jax.ShapeDtypeStruct((S, DFOX), F32),
                   jax.ShapeDtypeStruct((NPAIR, S, 2), F32)] + ex_out_shape,
        scratch_shapes=[pltpu.VMEM((2 * FOX_PAIRS, S, LANES), BF16), pltpu.VMEM((FOX_PAIRS, S, LANES), BF16)] + ex_scratch,
        compiler_params=_cparams(),
    )(*([proj] * (3 * FOX_PAIRS)), gates, w2, *ex_in)
    return res[0], res[1], res[2], res[3:]


def _fox_norm_bwd(o, dmix, w2):
    def body(o_ref, g_ref, w_ref, do_ref, dl_ref, dw_ref):
        hp, qi = pl.program_id(0), pl.program_id(1)
        masks = [_head_mask(0), _head_mask(1)]
        ov = o_ref[...]
        g = g_ref[...]
        r = _head_rms(ov, masks)
        gw = g * w_ref[...]
        gwo = gw * ov
        mean = [jnp.sum(jnp.where(mk, gwo, 0.0), axis=1, keepdims=True) * (1.0 / FHD) for mk in masks]
        do = r * gw - ov * (r * r * r) * jnp.where(masks[0], mean[0], mean[1])
        do_ref[...] = do.astype(BF16)
        doo = do * ov
        dl_ref[...] = _pair_cols(*[jnp.sum(jnp.where(mk, doo, 0.0), axis=1, keepdims=True) for mk in masks])

        @pl.when((hp == 0) & (qi == 0))
        def _():
            dw_ref[...] = jnp.zeros_like(dw_ref)

        dw_ref[...] += jnp.sum(g * ov * r, axis=0, keepdims=True)

        @pl.when((hp == NPAIR - 1) & (qi == NQ - 1))
        def _():
            dw = dw_ref[...]
            dw_ref[...] = dw + pltpu.roll(dw, FHD, 1)

    blk = pl.BlockSpec((BQ, LANES), lambda hp, i: (i, hp))
    vec = pl.BlockSpec((1, LANES), lambda hp, i: (0, 0))
    return pl.pallas_call(
        body, name="fox_norm_bwd", grid=(NPAIR, NQ), in_specs=[blk, blk, vec],
        out_specs=[blk, pl.BlockSpec((None, BQ, 2), lambda hp, i: (hp, i, 0)), vec],
        out_shape=[jax.ShapeDtypeStruct((S, DFOX), BF16), jax.ShapeDtypeStruct((NPAIR, S, 2), F32),
                   jax.ShapeDtypeStruct((1, LANES), F32)],
        compiler_params=_cparams(),
    )(o, dmix, w2)


def _fox_bwd(proj, do, gates, lse, delta, exchange=None):
    ex_in, ex_in_specs, ex_out_specs, ex_out_shape, ex_scratch = _hosted(exchange)

    pg = FOX_PAIRS_BWD
    n_in = 3 * pg + 4
    heads = [(pp, e) for pp in range(pg) for e in range(2)]

    def body(*refs):
        qkv_refs = refs[:3 * pg]
        do_ref, g_ref, lse_ref, dl_ref = refs[3 * pg:n_in]
        dproj_ref, dc_ref = refs[n_in + len(ex_in):n_in + 2 + len(ex_in)]
        qa_ref, dq_ref = refs[n_in + 2 + len(ex_in) + len(ex_out_shape):n_in + 4 + len(ex_in) + len(ex_out_shape)]
        ex_refs = refs[n_in:n_in + len(ex_in)] + refs[n_in + 2 + len(ex_in):n_in + 2 + len(ex_in) + len(ex_out_shape)] + refs[-2:]
        grp, kj = pl.program_id(0), pl.program_id(1)

        def head_index(pp, e):
            return 2 * (pg * grp + pp) + e

        if exchange is not None:
            @pl.when((grp == 0) & (kj == 0))
            def _():
                exchange.start(*exchange.split(ex_refs))

        @pl.when(kj == 0)
        def _():
            gt = g_ref[...]
            for pp in range(pg):
                qv = qkv_refs[3 * pp][...]
                for e in range(2):
                    qa_ref[2 * pp + e] = _fox_operand(qv, e, _lane_col(gt, head_index(pp, e)), True)
            dq_ref[...] = jnp.zeros_like(dq_ref)

        @pl.when((grp == 0) & (kj == 0))
        def _():
            dc_ref[...] = jnp.zeros_like(dc_ref)

        masks = [_head_mask(0), _head_mask(1)]
        krows = pl.ds(pl.multiple_of(kj * BQ, BQ), BQ)
        gk = g_ref[krows, :]
        kas = [_fox_operand(qkv_refs[3 * pp + 1][...], e, _lane_col(gk, head_index(pp, e)), False) for pp, e in heads]
        vbs = [qkv_refs[3 * pp + 2][...].astype(BF16) for pp in range(pg)]
        lane = lax.broadcasted_iota(jnp.int32, (BQ, LANES), 1)
        n = range(len(heads))

        def block(qi, carry, diagonal):
            dks, dvs, css = carry
            rows = pl.ds(pl.multiple_of(qi * BQ, BQ), BQ)
            qa = [qa_ref[i, rows, :] for i in n]
            s = [_dot(qa[i], kas[i], 1, 1) for i in n]
            if diagonal:
                s = [jnp.where(_causal_block(), s[i], -jnp.inf) for i in n]
            dov = [do_ref[rows, pp * LANES:(pp + 1) * LANES] for pp in range(pg)]
            doe = [jnp.where(masks[e], dov[pp], jnp.zeros_like(dov[pp])) for pp, e in heads]
            lse2 = [lse_ref[pp, rows, :] for pp in range(pg)]
            dl2 = [dl_ref[pp, rows, :] for pp in range(pg)]
            p = [jnp.exp(s[i] - _lane_col(lse2[heads[i][0]], heads[i][1])) for i in n]
            dp = [_dot(doe[i], vbs[heads[i][0]], 1, 1) for i in n]
            ds = [p[i] * (dp[i] - _lane_col(dl2[heads[i][0]], heads[i][1])) for i in n]
            dv_part = [_dot(p[i], doe[i], 0, 0) for i in n]
            dk_part = [_dot(ds[i], jnp.where(masks[heads[i][1]], qa[i], jnp.zeros_like(qa[i])), 0, 0) for i in n]
            dq_part = [jnp.where(masks[heads[i][1]], _dot(ds[i], kas[i]), 0.0) for i in n]
            css = tuple(css[i] + jnp.sum(ds[i], axis=0, keepdims=True) for i in n)
            dc = jnp.zeros((BQ, LANES), F32)
            for i in n:
                dc = dc + jnp.where(lane == head_index(*heads[i]), jnp.sum(ds[i], axis=1, keepdims=True), 0.0)
            for pp in range(pg):
                dq_ref[pp, rows, :] += (dq_part[2 * pp] + dq_part[2 * pp + 1]) * FOX_SCALE
            dc_ref[rows, :] += dc
            dks = tuple(dks[pp] + dk_part[2 * pp] + dk_part[2 * pp + 1] for pp in range(pg))
            dvs = tuple(dvs[pp] + dv_part[2 * pp] + dv_part[2 * pp + 1] for pp in range(pg))
            return dks, dvs, css

        zero = jnp.zeros((BQ, LANES), F32)
        first = block(kj, ((zero,) * pg, (zero,) * pg, (jnp.zeros((1, BQ), F32),) * len(heads)), True)
        dks, dvs, css = lax.fori_loop(kj + 1, NQ, lambda qi, carry: block(qi, carry, False), first)
        r = lax.broadcasted_iota(jnp.int32, (BQ, BQ), 0)
        c = lax.broadcasted_iota(jnp.int32, (BQ, BQ), 1)
        dcol = jnp.zeros((BQ, LANES), F32)
        for i in n:
            col = jnp.sum(jnp.where(r == c, css[i], 0.0), axis=1, keepdims=True)
            dcol = dcol + jnp.where(lane == head_index(*heads[i]), col, 0.0)
        dc_ref[krows, :] -= dcol
        for pp in range(pg):
            base = 3 * pp * LANES
            dproj_ref[krows, base + LANES:base + 2 * LANES] = dks[pp].astype(BF16)
            dproj_ref[krows, base + 2 * LANES:base + 3 * LANES] = dvs[pp].astype(BF16)

        @pl.when(kj == NQ - 1)
        def _():
            for pp in range(pg):
                dproj_ref[:, 3 * pp * LANES:(3 * pp + 1) * LANES] = dq_ref[pp].astype(BF16)

        if exchange is not None:
            @pl.when((grp == NPAIR // pg - 1) & (kj == NQ - 1))
            def _():
                exchange.finish(*exchange.split(ex_refs))

    qkv_specs = []
    for pp in range(pg):
        qkv_specs.append(pl.BlockSpec((S, LANES), lambda g, j, pp=pp: (0, 3 * (pg * g + pp))))
        qkv_specs.append(pl.BlockSpec((BQ, LANES), lambda g, j, pp=pp: (j, 3 * (pg * g + pp) + 1)))
        qkv_specs.append(pl.BlockSpec((BQ, LANES), lambda g, j, pp=pp: (j, 3 * (pg * g + pp) + 2)))
    pair = pl.BlockSpec((pg, S, 2), lambda g, j: (g, 0, 0))
    res = pl.pallas_call(
        body, name="fox_bwd", grid=(NPAIR // pg, NQ),
        in_specs=qkv_specs + [pl.BlockSpec((S, pg * LANES), lambda g, j: (0, g)), pl.BlockSpec((S, LANES), lambda g, j: (0, 0)),
                              pair, pair] + ex_in_specs,
        out_specs=[pl.BlockSpec((S, 3 * pg * LANES), lambda g, j: (0, g)), pl.BlockSpec((S, LANES), lambda g, j: (0, 0))]
        + ex_out_specs,
        out_shape=[jax.ShapeDtypeStruct((S, DPROJ_PAD), BF16), jax.ShapeDtypeStruct((S, LANES), F32)] + ex_out_shape,
        scratch_shapes=[pltpu.VMEM((2 * pg, S, LANES), BF16), pltpu.VMEM((pg, S, LANES), F32)] + ex_scratch,
        compiler_params=_cparams(),
    )(*([proj] * (3 * pg)), do, gates, lse, delta, *ex_in)
    return res[0], res[1], res[2:]


NQKV = 3 * NGH
GDN_QSCALE = GHD ** -0.5


def _shift_down(x, s):
    if s == 0:
        return x
    row = lax.broadcasted_iota(jnp.int32, x.shape, 0)
    return jnp.where(row >= s, pltpu.roll(x, s, 0), 0.0)


def _shift_up(x, s):
    if s == 0:
        return x
    n = x.shape[0]
    row = lax.broadcasted_iota(jnp.int32, x.shape, 0)
    return jnp.where(row < n - s, pltpu.roll(x, n - s, 0), 0.0)


def _conv_pre(xv, wv):
    pre = xv * wv[CONV_K - 1:CONV_K, :]
    for j in range(CONV_K - 1):
        pre = pre + _shift_down(xv, CONV_K - 1 - j) * wv[j:j + 1, :]
    return pre


def _l2_factors(b):
    return b < 2 * NGH, jnp.where(b < NGH, GDN_QSCALE, 1.0)


def _gdn_pre(proj, conv_w):
    def body(x_ref, w_ref, o_ref):
        b = pl.program_id(0)
        c = _silu(_conv_pre(x_ref[...], w_ref[...]))
        normed, scale = _l2_factors(b)
        rs = lax.rsqrt(jnp.sum(c * c, axis=-1, keepdims=True) + EPS)
        o_ref[...] = c * jnp.where(normed, rs, 1.0) * scale

    return pl.pallas_call(
        body, name="gdn_pre", grid=(NQKV,),
        in_specs=[pl.BlockSpec((S, GHD), lambda b: (0, BLK_GDN + b)), pl.BlockSpec((CONV_K, GHD), lambda b: (0, b))],
        out_specs=pl.BlockSpec((S, GHD), lambda b: (0, b)),
        out_shape=jax.ShapeDtypeStruct((S, NQKV * GHD), F32), compiler_params=_cparams(),
    )(proj, conv_w)


def _gdn_pre_bwd(proj, conv_w, dqkv, dproj):
    def body(x_ref, w_ref, dy_ref, dproj_in, dx_ref, dw_ref):
        del dproj_in
        b = pl.program_id(0)
        xv = x_ref[...]
        wv = w_ref[...]
        pre = _conv_pre(xv, wv)
        sig = _sigmoid(pre)
        c = pre * sig
        normed, scale = _l2_factors(b)
        g = dy_ref[...] * scale
        rs = lax.rsqrt(jnp.sum(c * c, axis=-1, keepdims=True) + EPS)
        dc_n = rs * g - c * (rs * rs * rs) * jnp.sum(g * c, axis=-1, keepdims=True)
        dc = jnp.where(normed, dc_n, g)
        dpre = dc * sig * (1.0 + pre * (1.0 - sig))
        dx = dpre * wv[CONV_K - 1:CONV_K, :]
        for j in range(CONV_K - 1):
            dx = dx + _shift_up(dpre, CONV_K - 1 - j) * wv[j:j + 1, :]
        dx_ref[...] = dx.astype(BF16)
        for j in range(CONV_K):
            dw_ref[j:j + 1, :] = jnp.sum(dpre * _shift_down(xv, CONV_K - 1 - j), axis=0, keepdims=True)

    return pl.pallas_call(
        body, name="gdn_pre_bwd", grid=(NQKV,),
        in_specs=[pl.BlockSpec((S, GHD), lambda b: (0, BLK_GDN + b)), pl.BlockSpec((CONV_K, GHD), lambda b: (0, b)),
                  pl.BlockSpec((None, S, GHD), lambda b: (b // NGH, 0, b % NGH)), pl.BlockSpec(memory_space=pl.ANY)],
        out_specs=[pl.BlockSpec((S, GHD), lambda b: (0, BLK_GDN + b)), pl.BlockSpec((CONV_K, GHD), lambda b: (0, b))],
        out_shape=[jax.ShapeDtypeStruct((S, DPROJ_PAD), BF16), jax.ShapeDtypeStruct((CONV_K, NQKV * GHD), F32)],
        input_output_aliases={3: 0}, compiler_params=_cparams(),
    )(proj, conv_w, dqkv, dproj)


CB = 4
NCB = NCH // CB


def _chunk_prep(qs, ks, vs, gcols, bcols, t_saved=None):
    n = range(len(qs))
    r = lax.broadcasted_iota(jnp.int32, (CHUNK, CHUNK), 0)
    c = lax.broadcasted_iota(jnp.int32, (CHUNK, CHUNK), 1)
    incl = c <= r
    eye = (r == c).astype(F32)
    grow = [jnp.sum(gcols[i] * eye, axis=0, keepdims=True) for i in n]
    gc_col = [jnp.sum(jnp.where(incl, grow[i], 0.0), axis=1, keepdims=True) for i in n]
    gc_row = [jnp.sum(jnp.where(r <= c, gcols[i], 0.0), axis=0, keepdims=True) for i in n]
    decay = [jnp.exp(jnp.where(incl, gc_col[i] - gc_row[i], -jnp.inf)) for i in n]
    kb = [ks[i] * bcols[i] for i in n]
    vb = [vs[i] * bcols[i] for i in n]
    kk = [_mm_nt(kb[i], ks[i]) for i in n]
    m = [jnp.where(c < r, kk[i] * decay[i], 0.0) for i in n]
    if t_saved is None:
        t_inv = [eye - m[i] for i in n]
        p = [_hdot(m[i], m[i]) for i in n]
        for step in range(5):
            t_inv = [t_inv[i] + _hdot(t_inv[i], p[i]) for i in n]
            if step < 4:
                p = [_hdot(p[i], p[i]) for i in n]
    else:
        t_inv = [_saved_inverse(m[i], t_saved[i]) for i in n]
    egc = [jnp.exp(gc_col[i]) for i in n]
    u = [_mm_nn(t_inv[i], vb[i]) for i in n]
    w = [_mm_nn(t_inv[i], kb[i] * egc[i]) for i in n]
    qk = [_mm_nt(qs[i], ks[i]) for i in n]
    gc_last = [gc_col[i][CHUNK - 1:CHUNK, :] for i in n]
    return [(u[i], w[i], qk[i] * decay[i], qs[i] * egc[i], ks[i] * jnp.exp(gc_last[i] - gc_col[i]), jnp.exp(gc_last[i]),
             t_inv[i]) for i in n]


def _prep_specs():
    rows = CB * CHUNK
    qs = pl.BlockSpec((rows, GHD), lambda i, h: (i, h))
    ks = pl.BlockSpec((rows, GHD), lambda i, h: (i, NGH + h))
    vs = pl.BlockSpec((rows, GHD), lambda i, h: (i, 2 * NGH + h))
    gs = pl.BlockSpec((rows, LANES), lambda i, h: (i, 0))
    a_s = pl.BlockSpec((None, rows, CHUNK), lambda i, h: (h, i, 0))
    gl_s = pl.BlockSpec((None, CB, 1, LANES), lambda i, h: (h, i, 0, 0))
    return qs, ks, vs, gs, a_s, gl_s


def _gdn_prep(qkv, gates, exchange=None):
    ex_in, ex_in_specs, ex_out_specs, ex_out_shape, ex_scratch = _hosted(exchange)

    def body(*refs):
        q_ref, k_ref, v_ref, g_ref = refs[:4]
        u_ref, w_ref, qd_ref, kd_ref, a_ref, gl_ref, t_ref = refs[4 + len(ex_in):11 + len(ex_in)]
        ex_refs = refs[4:4 + len(ex_in)] + refs[11 + len(ex_in):]
        h = pl.program_id(1)

        if exchange is not None:
            @pl.when((pl.program_id(0) == 0) & (h == 0))
            def _():
                exchange.start(*exchange.split(ex_refs))

        chunks = [pl.ds(cidx * CHUNK, CHUNK) for cidx in range(CB)]
        gts = [g_ref[rows, :] for rows in chunks]
        outs = _chunk_prep([q_ref[rows, :] for rows in chunks], [k_ref[rows, :] for rows in chunks],
                           [v_ref[rows, :] for rows in chunks], [_lane_col(gt, LANE_G + h) for gt in gts],
                           [_lane_col(gt, LANE_BETA + h) for gt in gts])
        for cidx, rows in enumerate(chunks):
            u, w, a, qd, kd, gl, t_inv = outs[cidx]
            u_ref[rows, :] = u
            w_ref[rows, :] = w
            qd_ref[rows, :] = qd
            kd_ref[rows, :] = kd
            a_ref[rows, :] = a
            t_ref[rows, :] = t_inv
            gl_ref[cidx] = jnp.broadcast_to(gl, (1, LANES))

        if exchange is not None:
            @pl.when((pl.program_id(0) == NCB - 1) & (h == NGH - 1))
            def _():
                exchange.finish(*exchange.split(ex_refs))

    qs, ks, vs, gs, a_s, gl_s = _prep_specs()
    tok = jax.ShapeDtypeStruct((S, DGDN), F32)
    sq = jax.ShapeDtypeStruct((NGH, S, CHUNK), F32)
    res = pl.pallas_call(
        body, name="gdn_prep", grid=(NCB, NGH), in_specs=[qs, ks, vs, gs] + ex_in_specs,
        out_specs=[qs, qs, qs, qs, a_s, gl_s, a_s] + ex_out_specs,
        out_shape=[tok, tok, tok, tok, sq, jax.ShapeDtypeStruct((NGH, NCH, 1, LANES), F32), sq] + ex_out_shape,
        scratch_shapes=ex_scratch, compiler_params=_cparams(),
    )(qkv, qkv, qkv, gates, *ex_in)
    return res[:7], res[7:]


def _gdn_prep_bwd(qkv, gates, t_inv, du, dw, dqd, dkd, da, dgl):
    def body(q_ref, k_ref, v_ref, g_ref, t_ref, du_ref, dw_ref, dqd_ref, dkd_ref, da_ref, dgl_ref, dqkv_ref, dg_ref):
        h = pl.program_id(1)

        @pl.when(h == 0)
        def _():
            dg_ref[...] = jnp.zeros_like(dg_ref)

        lane = lax.broadcasted_iota(jnp.int32, (CHUNK, LANES), 1)
        chunks = [pl.ds(cidx * CHUNK, CHUNK) for cidx in range(CB)]
        gts = [g_ref[rows, :] for rows in chunks]
        t_saved = [t_ref[rows, :] for rows in chunks]
        _, vjp = jax.vjp(lambda *args: [o[:6] for o in _chunk_prep(*args, t_saved=t_saved)],
                         [q_ref[rows, :] for rows in chunks], [k_ref[rows, :] for rows in chunks],
                         [v_ref[rows, :] for rows in chunks], [_lane_col(gt, LANE_G + h) for gt in gts],
                         [_lane_col(gt, LANE_BETA + h) for gt in gts])
        dqs, dks, dvs, dgcs, dbcs = vjp([(du_ref[rows, :], dw_ref[rows, :], da_ref[rows, :], dqd_ref[rows, :],
                                          dkd_ref[rows, :], dgl_ref[cidx][:, 0:1]) for cidx, rows in enumerate(chunks)])
        for cidx, rows in enumerate(chunks):
            dq, dk, dv, dgc, dbc = dqs[cidx], dks[cidx], dvs[cidx], dgcs[cidx], dbcs[cidx]
            dqkv_ref[0, rows, :] = dq
            dqkv_ref[1, rows, :] = dk
            dqkv_ref[2, rows, :] = dv
            dg_ref[rows, :] += jnp.where(lane == LANE_G + h, dgc, 0.0) + jnp.where(lane == LANE_BETA + h, dbc, 0.0)

    qs, ks, vs, gs, a_s, gl_s = _prep_specs()
    return pl.pallas_call(
        body, name="gdn_prep_bwd", grid=(NCB, NGH), in_specs=[qs, ks, vs, gs, a_s, qs, qs, qs, qs, a_s, gl_s],
        out_specs=[pl.BlockSpec((3, CB * CHUNK, GHD), lambda i, h: (0, i, h)), gs],
        out_shape=[jax.ShapeDtypeStruct((3, S, DGDN), F32), jax.ShapeDtypeStruct((S, LANES), F32)],
        compiler_params=_cparams(),
    )(qkv, qkv, qkv, gates, t_inv, du, dw, dqd, dkd, da, dgl)


def _scan_specs():
    hs = pl.BlockSpec((S, GHD), lambda h: (0, h))
    a_s = pl.BlockSpec((None, S, CHUNK), lambda h: (h, 0, 0))
    gl_s = pl.BlockSpec((None, NCH, 1, LANES), lambda h: (h, 0, 0, 0))
    st_s = pl.BlockSpec((None, NCH, GHD, GHD), lambda h: (h, 0, 0, 0))
    gz_s = pl.BlockSpec((S, GHD), lambda h: (0, BLK_GZ + h))
    mix_s = pl.BlockSpec((S, GHD), lambda h: (0, NPAIR + h))
    return hs, a_s, gl_s, st_s, gz_s, mix_s


def _gdn_scan(u, w, qd, kd, a, gl, proj, w_norm, mix):
    def body(u_ref, w_ref, qd_ref, kd_ref, a_ref, gl_ref, z_ref, wn_ref, mix_in, mix_ref, o_ref, st_ref):
        del mix_in

        def step(ci, state):
            rows = pl.ds(pl.multiple_of(ci * CHUNK, CHUNK), CHUNK)
            st_ref[ci] = state
            vn = u_ref[rows, :] - _dot(w_ref[rows, :], state)
            o_ref[rows, :] = _dot(qd_ref[rows, :], state) + _dot(a_ref[rows, :], vn)
            return state * gl_ref[ci] + _dot(kd_ref[rows, :], vn, 0, 0)

        lax.fori_loop(0, NCH, step, jnp.zeros((GHD, GHD), F32))
        ov = o_ref[...]
        mix_ref[...] = (ov * _rms_scale(ov) * wn_ref[...] * _silu(z_ref[...])).astype(BF16)

    hs, a_s, gl_s, st_s, gz_s, mix_s = _scan_specs()
    return pl.pallas_call(
        body, name="gdn_scan", grid=(NGH,),
        in_specs=[hs, hs, hs, hs, a_s, gl_s, gz_s, pl.BlockSpec((1, GHD), lambda h: (0, 0)), pl.BlockSpec(memory_space=pl.ANY)],
        out_specs=[mix_s, hs, st_s],
        out_shape=[jax.ShapeDtypeStruct((S, D), BF16), jax.ShapeDtypeStruct((S, DGDN), F32),
                   jax.ShapeDtypeStruct((NGH, NCH, GHD, GHD), F32)],
        input_output_aliases={8: 0}, compiler_params=_cparams(),
    )(u, w, qd, kd, a, gl, proj, w_norm, mix)


def _gdn_scan_bwd(dmix, o, proj, w_norm, u, w, qd, kd, a, gl, states, dproj):
    def body(dy_ref, o_ref, z_ref, wn_ref, u_ref, w_ref, qd_ref, kd_ref, a_ref, gl_ref, st_ref, dproj_in,
             dz_ref, du_ref, dw_ref, dqd_ref, dkd_ref, da_ref, dgl_ref, dwn_ref, do_ref):
        del dproj_in
        h = pl.program_id(0)
        ov = o_ref[...]
        zv = z_ref[...]
        wn = wn_ref[...]
        g = dy_ref[...]
        sig = _sigmoid(zv)
        dz_ref[...] = (g * (ov * _rms_scale(ov) * wn) * sig * (1.0 + zv * (1.0 - sig))).astype(BF16)
        do, dwt = _rms_bwd(ov, wn, g * zv * sig)
        do_ref[...] = do

        @pl.when(h == 0)
        def _():
            dwn_ref[...] = jnp.zeros_like(dwn_ref)

        dwn_ref[...] += jnp.sum(dwt, axis=0, keepdims=True)

        def step(t, dstate):
            ci = NCH - 1 - t
            rows = pl.ds(pl.multiple_of(ci * CHUNK, CHUNK), CHUNK)
            state = st_ref[ci]
            dov = do_ref[rows, :]
            wv = w_ref[rows, :]
            kdv = kd_ref[rows, :]
            vn = u_ref[rows, :] - _dot(wv, state)
            dvn = _dot(a_ref[rows, :], dov, 0, 0) + _dot(kdv, dstate)
            da_ref[rows, :] = _dot(dov, vn, 1, 1)
            dqd_ref[rows, :] = _dot(dov, state, 1, 1)
            dkd_ref[rows, :] = _dot(vn, dstate, 1, 1)
            dgl = jnp.sum(jnp.sum(dstate * state, axis=1, keepdims=True), axis=0, keepdims=True)
            dgl_ref[ci] = jnp.broadcast_to(dgl, (1, LANES))
            du_ref[rows, :] = dvn
            dw_ref[rows, :] = -_dot(dvn, state, 1, 1)
            return dstate * gl_ref[ci] + _dot(qd_ref[rows, :], dov, 0, 0) - _dot(wv, dvn, 0, 0)

        lax.fori_loop(0, NCH, step, jnp.zeros((GHD, GHD), F32))

    hs, a_s, gl_s, st_s, gz_s, mix_s = _scan_specs()
    vec = pl.BlockSpec((1, GHD), lambda h: (0, 0))
    tok = jax.ShapeDtypeStruct((S, DGDN), F32)
    return pl.pallas_call(
        body, name="gdn_scan_bwd", grid=(NGH,),
        in_specs=[mix_s, hs, gz_s, vec, hs, hs, hs, hs, a_s, gl_s, st_s, pl.BlockSpec(memory_space=pl.ANY)],
        out_specs=[gz_s, hs, hs, hs, hs, a_s, gl_s, vec],
        out_shape=[jax.ShapeDtypeStruct((S, DPROJ_PAD), BF16), tok, tok, tok, tok,
                   jax.ShapeDtypeStruct((NGH, S, CHUNK), F32), jax.ShapeDtypeStruct((NGH, NCH, 1, LANES), F32),
                   jax.ShapeDtypeStruct((1, GHD), F32)],
        input_output_aliases={11: 0}, scratch_shapes=[pltpu.VMEM((S, GHD), F32)], compiler_params=_cparams(),
    )(dmix, o, proj, w_norm, u, w, qd, kd, a, gl, states, dproj)


def _place():
    return lax.axis_index("x"), lax.axis_index("y"), lax.axis_index("c")


def _other_chips(x, y):
    return [(1 - x, y), (x, 1 - y), (1 - x, 1 - y)]


HBM = pl.BlockSpec(memory_space=pltpu.HBM)
VMEM = pl.BlockSpec(memory_space=pltpu.VMEM)


def _half_rows(ref_or_rows, half):
    rows = ref_or_rows // 2
    return pl.ds(pl.multiple_of(half * rows, rows), rows)


class _Exchange:
    def __init__(self, inputs, out_shape, n_sems, start, finish):
        self.inputs, self.out_shape, self.n_sems, self.start, self.finish = inputs, out_shape, n_sems, start, finish

    def sem_shapes(self):
        return [pltpu.SemaphoreType.DMA((self.n_sems,)), pltpu.SemaphoreType.DMA((self.n_sems,))]

    def split(self, refs):
        n_in, n_out = len(self.inputs), len(self.out_shape)
        return refs[:n_in], refs[n_in:n_in + n_out], refs[n_in + n_out], refs[n_in + n_out + 1]


def _run_exchange(ex, name):
    def body(*refs):
        parts = ex.split(refs)
        ex.start(*parts)
        ex.finish(*parts)

    return pl.pallas_call(
        body, name=name, in_specs=[HBM] * len(ex.inputs), out_specs=[HBM] * len(ex.out_shape), out_shape=ex.out_shape,
        scratch_shapes=ex.sem_shapes(), compiler_params=_cparams(),
    )(*ex.inputs)


def _allgather_exchange(shards, whole=()):
    n, nw = len(shards), len(whole)

    def plan(src, outs, send_sems, recv_sems):
        x, y, c = _place()
        chips = _other_chips(x, y)
        chip_ids = [2 * ch[0] + ch[1] for ch in chips]

        def copy(a, k, chip_index, half, to, from_src):
            rows = _half_rows(src[a].shape[0], half)
            dst = outs[a].at[chip_index, rows]
            return pltpu.make_async_remote_copy(
                src_ref=src[a].at[rows] if from_src else dst, dst_ref=dst, send_sem=send_sems.at[6 * a + k],
                recv_sem=recv_sems.at[6 * a + k], device_id=to, device_id_type=MESH)

        def whole_copy(b, k, chip_index, to):
            return pltpu.make_async_remote_copy(
                src_ref=src[n + b], dst_ref=outs[n + b].at[chip_index], send_sem=send_sems.at[6 * n + 3 * b + k],
                recv_sem=recv_sems.at[6 * n + 3 * b + k], device_id=to, device_id_type=MESH)

        me, sibling = (x, y, c), (x, y, 1 - c)
        first = [copy(a, j, 2 * x + y, c, (*chips[j], c), True) for a in range(n) for j in range(3)]
        first += [whole_copy(b, j, 2 * x + y, (*chips[j], c)) for b in range(nw) for j in range(3)]
        landing = [copy(a, j, chip_ids[j], c, me, False) for a in range(n) for j in range(3)]
        passed = [copy(a, 3 + j, chip_ids[j], c, sibling, False) for a in range(n) for j in range(3)]
        arriving = [copy(a, 3 + j, chip_ids[j], 1 - c, me, False) for a in range(n) for j in range(3)]
        arriving += [whole_copy(b, j, chip_ids[j], me) for b in range(nw) for j in range(3)]
        return first, landing, passed, arriving

    def start(*refs):
        for cp in plan(*refs)[0]:
            cp.start()

    def finish(*refs):
        first, landing, passed, arriving = plan(*refs)
        for lands, onward in zip(landing, passed):
            lands.wait_recv()
            onward.start()
        for cp in arriving:
            cp.wait_recv()
        for cp in first + passed:
            cp.wait_send()

    out_shape = [jax.ShapeDtypeStruct((NCHIP,) + s.shape, s.dtype) for s in list(shards) + list(whole)]
    return _Exchange(list(shards) + list(whole), out_shape, 6 * n + 3 * nw, start, finish)


def _with_own(gathered, own):
    x, y, _ = _place()
    return lax.dynamic_update_index_in_dim(gathered, own, 2 * x + y, axis=0)


def _simple_exchange(inputs, out_shape, copies_of):
    def start(*refs):
        for cp in copies_of(*refs):
            cp.start()

    def finish(*refs):
        for cp in copies_of(*refs):
            cp.wait()

    return _Exchange(list(inputs), out_shape, len(out_shape) * 3, start, finish)


def _pair_exchange(grads):
    def copies_of(src, outs, send_sems, recv_sems):
        x, y, c = _place()
        return [pltpu.make_async_remote_copy(
            src_ref=src[a].at[:, _half_rows(src[a].shape[1], 1 - c)], dst_ref=outs[a], send_sem=send_sems.at[a],
            recv_sem=recv_sems.at[a], device_id=(x, y, 1 - c), device_id_type=MESH) for a in range(len(src))]

    return _simple_exchange(
        grads, [jax.ShapeDtypeStruct((g.shape[0], g.shape[1] // 2, g.shape[2]), g.dtype) for g in grads], copies_of)


def _pair_sum(grads, theirs, name):
    n = len(grads)

    def body(*refs):
        south = lax.axis_index("c") == 0
        for a in range(n):
            g = refs[a][...]
            half = g.shape[0] // 2
            mine = jnp.where(south, g[:half], g[half:])
            refs[2 * n + a][...] = (mine.astype(F32) + refs[n + a][...].astype(F32)).astype(BF16)

    def specs(arrs):
        return [pl.BlockSpec((None,) + g.shape[1:], lambda j: (j, 0, 0)) for g in arrs]

    return pl.pallas_call(
        body, name=name, grid=(NCHIP,), in_specs=specs(grads) + specs(theirs), out_specs=specs(theirs),
        out_shape=[jax.ShapeDtypeStruct(g.shape, BF16) for g in theirs], compiler_params=_cparams(),
    )(*grads, *theirs)


def _chip_exchange(parts):
    def copies_of(src, outs, send_sems, recv_sems):
        x, y, c = _place()
        return [pltpu.make_async_remote_copy(
            src_ref=src[a].at[2 * chip[0] + chip[1]], dst_ref=outs[a].at[k], send_sem=send_sems.at[3 * a + k],
            recv_sem=recv_sems.at[3 * a + k], device_id=(*chip, c), device_id_type=MESH)
            for a in range(len(src)) for k, chip in enumerate(_other_chips(x, y))]

    return _simple_exchange(parts, [jax.ShapeDtypeStruct((NCHIP - 1,) + p.shape[1:], p.dtype) for p in parts], copies_of)


def _chip_sum(parts, received):
    n = len(parts)
    steps = 4

    def body(*refs):
        chip = 2 * lax.axis_index("x") + lax.axis_index("y")
        for a in range(n):
            p, r = refs[a], refs[n + a]
            own = jnp.where(chip == 0, p[0], jnp.where(chip == 1, p[1], jnp.where(chip == 2, p[2], p[3])))
            refs[2 * n + a][...] = ((own.astype(F32) + r[0].astype(F32)) + r[1].astype(F32)) + r[2].astype(F32)

    def specs(arrs):
        return [pl.BlockSpec((g.shape[0], g.shape[1] // steps, g.shape[2]), lambda i: (0, i, 0)) for g in arrs]

    out_specs = [pl.BlockSpec((g.shape[1] // steps, g.shape[2]), lambda i: (i, 0)) for g in parts]
    return pl.pallas_call(
        body, name="grads_chip_sum", grid=(steps,), in_specs=specs(parts) + specs(received), out_specs=out_specs,
        out_shape=[jax.ShapeDtypeStruct(g.shape[1:], F32) for g in parts], compiler_params=_cparams(),
    )(*parts, *received)


def _pair_share(halves):
    def copies_of(src, outs, send_sems, recv_sems):
        x, y, c = _place()
        return [pltpu.make_async_remote_copy(
            src_ref=src[a], dst_ref=outs[a], send_sem=send_sems.at[a], recv_sem=recv_sems.at[a],
            device_id=(x, y, 1 - c), device_id_type=MESH) for a in range(len(src))]

    return _simple_exchange(halves, [jax.ShapeDtypeStruct(h.shape, F32) for h in halves], copies_of)


def _adamw_math(w, g, m, v):
    nm = ADAM_B1 * m + (1.0 - ADAM_B1) * g
    nv = ADAM_B2 * v + (1.0 - ADAM_B2) * jnp.square(g)
    m_hat = nm / (1.0 - ADAM_B1 ** ADAM_STEP)
    v_hat = nv / (1.0 - ADAM_B2 ** ADAM_STEP)
    return -ADAM_LR * (m_hat / (jnp.sqrt(v_hat) + ADAM_EPS) + ADAM_WD * w), nm, nv


def _adamw_big(ws, g_mine, g_theirs, ms, vs):
    n = len(ws)
    steps = 8

    def body(*refs):
        own_half = (pl.program_id(0) // (steps // 2)) == lax.axis_index("c")
        for a in range(n):
            w = refs[a][...]
            g = jnp.where(own_half, refs[n + a][...], refs[2 * n + a][...])[:, :w.shape[1]]
            d, nm, nv = _adamw_math(w, g, refs[3 * n + a][...], refs[4 * n + a][...])
            refs[5 * n + a][...] = g
            refs[6 * n + a][...] = d
            refs[7 * n + a][...] = nm
            refs[8 * n + a][...] = nv

    specs = [pl.BlockSpec((w.shape[0] // steps, w.shape[1]), lambda i: (i, 0)) for w in ws]
    half_specs = [pl.BlockSpec((g.shape[0] // (steps // 2), g.shape[1]), lambda i: (i % (steps // 2), 0)) for g in g_mine]
    shapes = [jax.ShapeDtypeStruct(w.shape, F32) for w in ws]
    res = pl.pallas_call(
        body, name="adamw_big", grid=(steps,), in_specs=specs + half_specs * 2 + specs * 2, out_specs=specs * 4,
        out_shape=shapes * 4, compiler_params=_cparams(),
    )(*ws, *g_mine, *g_theirs, *ms, *vs)
    return res[:n], res[n:2 * n], res[2 * n:3 * n], res[3 * n:]


def _adamw_in(w, g_mine, g_theirs, m, v):
    half = D // 2

    def body(w_ref, gm_ref, gt_ref, m_ref, v_ref, g_out, d_out, nm_out, nv_out, g_ref):
        south = lax.axis_index("c") == 0
        g_ref[0:half, :] = jnp.where(south, gm_ref[...], gt_ref[...])
        g_ref[half:D, :] = jnp.where(south, gt_ref[...], gm_ref[...])
        g = g_ref[0:CW, :]
        d, nm, nv = _adamw_math(w_ref[...], g, m_ref[...], v_ref[...])
        g_out[...] = g
        d_out[...] = d
        nm_out[...] = nm
        nv_out[...] = nv

    spec = pl.BlockSpec((CW, LANES), lambda i: (0, i))
    half_spec = pl.BlockSpec((half, LANES), lambda i: (0, i))
    return pl.pallas_call(
        body, name="adamw_in", grid=(D // LANES,), in_specs=[spec, half_spec, half_spec, spec, spec], out_specs=[spec] * 4,
        out_shape=[jax.ShapeDtypeStruct((CW, D), F32)] * 4, scratch_shapes=[pltpu.VMEM((D, LANES), F32)],
        compiler_params=_cparams(),
    )(w, g_mine, g_theirs, m, v)


NORM_NAMES = ("pre_mix_norm", "post_mix_norm", "pre_mlp_norm", "post_mlp_norm")
SMALL_NAMES = NORM_NAMES + ("gdn_conv_w", "fox_f_bias", "gdn_dt_bias", "gdn_a_log", "fox_out_norm", "gdn_out_norm")
CONV_COLS = 3 * DGDN // NCHIP


def _small_allreduce(d_norms, d_conv, sums, d_fox_norm, d_gdn_norm):
    def body(*refs):
        dn_ref, dconv_ref, sums_ref, dfn_ref, dgn_ref = refs[:5]
        outs = refs[5:10]
        g_norms, g_conv, g_sums, g_fn, g_gn, send_sems, recv_sems, local_sem = refs[10:]
        x, y, c = _place()
        me = 4 * x + 2 * y + c
        g_norms[me] = dn_ref[...]
        g_sums[me] = sums_ref[...]
        g_fn[me] = dfn_ref[...]
        g_gn[me] = dgn_ref[...]

        def conv_cols(chip_index):
            return dconv_ref.at[:, pl.ds(pl.multiple_of(chip_index * CONV_COLS, LANES), CONV_COLS)]

        own = pltpu.make_async_copy(conv_cols(2 * x + y), g_conv.at[me], local_sem)
        own.start()
        copies = []
        for k in range(1, NDEV):
            px, py, pc = x ^ ((k >> 2) & 1), y ^ ((k >> 1) & 1), c ^ (k & 1)
            pairs = [(dn_ref, g_norms), (conv_cols(2 * px + py), g_conv), (sums_ref, g_sums), (dfn_ref, g_fn), (dgn_ref, g_gn)]
            for a, (src, dst) in enumerate(pairs):
                cp = pltpu.make_async_remote_copy(
                    src_ref=src, dst_ref=dst.at[me], send_sem=send_sems.at[5 * (k - 1) + a],
                    recv_sem=recv_sems.at[5 * (k - 1) + a], device_id=(px, py, pc), device_id_type=MESH)
                cp.start()
                copies.append(cp)
        own.wait()
        for cp in copies:
            cp.wait()

        def total(buf):
            acc = buf[0]
            for i in range(1, NDEV):
                acc = acc + buf[i]
            return acc

        for out, buf in zip(outs, (g_norms, g_conv, g_sums, g_fn, g_gn)):
            out[...] = total(buf)

    n_sem = 5 * (NDEV - 1)
    shapes = [(4, D), (CONV_K, CONV_COLS), (8, LANES), (1, LANES), (1, LANES)]
    return pl.pallas_call(
        body, name="small_allreduce", in_specs=[VMEM] * 5, out_specs=[VMEM] * 5,
        out_shape=[jax.ShapeDtypeStruct(s, F32) for s in shapes],
        scratch_shapes=[pltpu.VMEM((NDEV,) + s, F32) for s in shapes]
        + [pltpu.SemaphoreType.DMA((n_sem,)), pltpu.SemaphoreType.DMA((n_sem,)), pltpu.SemaphoreType.DMA],
        compiler_params=_cparams(),
    )(d_norms, d_conv, sums, d_fox_norm, d_gdn_norm)


def _small_adamw(totals, ws, ms, vs):
    n = len(SMALL_NAMES)

    def body(*refs):
        t_norms, t_conv, t_sums, t_fn, t_gn = [r[...] for r in refs[:5]]
        w_refs, m_refs, v_refs = refs[5:5 + n], refs[5 + n:5 + 2 * n], refs[5 + 2 * n:5 + 3 * n]
        outs = refs[5 + 3 * n:]
        grads = [t_norms[i:i + 1, :] for i in range(4)] + [
            t_conv, t_sums[0:1, 0:NFH], t_sums[1:2, 0:NGH], t_sums[2:3, 0:NGH], t_fn[:, 0:FHD], t_gn]
        for a in range(n):
            d, nm, nv = _adamw_math(w_refs[a][...], grads[a], m_refs[a][...], v_refs[a][...])
            outs[a][...] = grads[a]
            outs[n + a][...] = d
            outs[2 * n + a][...] = nm
            outs[3 * n + a][...] = nv

    def whole(arr):
        return pl.BlockSpec(arr.shape, lambda i: (0, 0))

    res = pl.pallas_call(
        body, name="small_adamw", grid=(1,), in_specs=[whole(t) for t in totals] + [whole(w) for w in ws] * 3,
        out_specs=[whole(w) for w in ws] * 4, out_shape=[jax.ShapeDtypeStruct(w.shape, F32) for w in ws] * 4,
        compiler_params=_cparams(),
    )(*totals, *ws, *ms, *vs)
    return res[:n], res[n:2 * n], res[2 * n:3 * n], res[3 * n:]


CW = DPROJ // NCHIP
PROJ_RUNS = tuple((part * DFOX + hp * LANES, part * DFOX + (hp + 1) * LANES, (3 * hp + part) * LANES)
                  for hp in range(NPAIR) for part in range(3)) + (
    (1536, 1544, BLK_SMALL * LANES), (1544, 3080, BLK_GDN * LANES), (3080, 3088, BLK_SMALL * LANES + 8),
    (3088, 3600, BLK_GZ * LANES))


def _to_padded_rows(gathered):
    pieces = []
    for lo, hi, at in sorted(PROJ_RUNS, key=lambda run: run[2]):
        if pieces and at > pieces[-1][0]:
            pieces.append((at, jnp.zeros((at - pieces[-1][0], D), gathered.dtype)))
        while lo < hi:
            j = lo // CW
            end = min(hi, (j + 1) * CW)
            pieces.append((at + end - lo, gathered[j, lo - j * CW:end - j * CW]))
            at, lo = at + end - lo, end
    pieces.append((DPROJ_PAD, jnp.zeros((DPROJ_PAD - pieces[-1][0], D), gathered.dtype)))
    return jnp.concatenate([p for _, p in pieces], axis=0)


def _from_padded_rows(w):
    blocks = []
    for j in range(NCHIP):
        pieces = []
        for lo, hi, at in sorted(PROJ_RUNS):
            a, b = max(lo, j * CW), min(hi, (j + 1) * CW)
            if a < b:
                pieces.append(w[at + a - lo:at + b - lo])
        blocks.append(jnp.concatenate(pieces + [jnp.zeros((D - CW, D), w.dtype)], axis=0))
    return jnp.stack(blocks)


def _local_step(x, target, win_p, late_weights, reduce_late, reduce_in, pre_mix_norm, fox_f_bias, fox_out_norm, conv_w,
                gdn_a_log, gdn_dt_bias, gdn_out_norm, post_mix_norm, pre_mlp_norm, post_mlp_norm):
    bias_vec = jnp.zeros((1, LANES), F32).at[0, 0:NFH].set(fox_f_bias).at[0, LANE_G:LANE_G + NGH].set(gdn_dt_bias)
    alog_vec = jnp.zeros((1, LANES), F32).at[0, LANE_G:LANE_G + NGH].set(gdn_a_log)
    w2 = jnp.concatenate([fox_out_norm, fox_out_norm], axis=1)

    h = _pre_norm(x, pre_mix_norm)
    proj = _matmul(h, win_p, tb=True, tm=2048, tn=768, tk=1024, name="mm_proj")
    gates = _gates(proj, bias_vec, alog_vec)
    mix, fox_o, lse, late_a = _fox_fwd(proj, gates, w2, exchange=late_weights[0])
    qkv = _gdn_pre(proj, conv_w)
    (u, w, qd, kd, a_intra, gl, t_inv), late_b = _gdn_prep(qkv, gates, exchange=late_weights[1])
    wout, wup3, wdown = late_weights[2](late_a, late_b)
    mix, gdn_raw, states = _gdn_scan(u, w, qd, kd, a_intra, gl, proj, gdn_out_norm, mix)
    mixed = _matmul(mix, wout, tm=2048, tk=1024, name="mm_out")
    x1, h2 = _post_mix(x, mixed, post_mix_norm, pre_mlp_norm)

    def relu2(acc):
        r = jnp.maximum(acc, 0.0)
        return acc, r * r

    up, act = _matmul(h2, wup3, b3=True, tm=1024, tn=1024, tk=1024, out_dtypes=(F32, BF16), epilogue=relu2, name="mm_up")
    y = _matmul(act, wdown, tm=2048, tk=1024, name="mm_down")
    dx2, dy, d_post_mlp, loss_row = _loss_head(x1, y, post_mlp_norm, target)

    dwdown = _matmul(act, dy, ta=True, tm=1024, tn=1024, tk=2048, out_dtypes=(BF16,), name="mm_dwdown")

    def relu2_bwd(acc, upv):
        return (acc * 2.0 * jnp.maximum(upv, 0.0),)

    dup = _matmul(dy, wdown, tb=True, tm=1024, tn=1024, tk=1024, out_dtypes=(BF16,), extra=(up,), epilogue=relu2_bwd,
                  name="mm_dact")
    dwup3 = _matmul(h2, dup, ta=True, tm=1024, tn=1024, tk=2048, out_dtypes=(BF16,), o3=True, name="mm_dwup")
    dh2 = _matmul(dup, wup3, tb=True, b3=True, tm=2048, tk=1024, name="mm_dh2")
    dx1, dmixed, d_pre_mlp, d_post_mix = _mid_bwd(dh2, x1, pre_mlp_norm, dx2, mixed, post_mix_norm)
    dwout = _matmul(mix, dmixed, ta=True, tm=1024, tn=1024, tk=2048, out_dtypes=(BF16,), name="mm_dwout")
    dmix = _matmul(dmixed, wout, tb=True, tm=2048, tk=1024, name="mm_dmix")

    dfox, delta, d_fox_norm = _fox_norm_bwd(fox_o, dmix, w2)
    dproj, dcum_fox, reduced_late = _fox_bwd(proj, dfox, gates, lse, delta, exchange=reduce_late(dwout, dwup3, dwdown))
    dproj, du, dw, dqd, dkd, da, dgl, d_gdn_norm = _gdn_scan_bwd(dmix, gdn_raw, proj, gdn_out_norm, u, w, qd, kd,
                                                                 a_intra, gl, states, dproj)
    dqkv, dgates_gdn = _gdn_prep_bwd(qkv, gates, t_inv, du, dw, dqd, dkd, da, dgl)
    dproj, d_conv = _gdn_pre_bwd(proj, conv_w, dqkv, dproj)
    dproj, sums = _gates_bwd(proj, bias_vec, alog_vec, dgates_gdn, dcum_fox, dproj)

    dwin_p = _matmul(dproj, h, ta=True, tm=1280, tn=1024, tk=2048, out_dtypes=(BF16,), name="mm_dwin")
    exchange_in = reduce_in(dwin_p)
    dh = _matmul(dproj, win_p, tm=2048, tk=1280, name="mm_dh", exchange=exchange_in)
    dh, reduced_in = dh if exchange_in is not None else (dh, [])
    grad_x, d_pre_mix = _pre_norm_bwd(dh, x, pre_mix_norm, dx1)

    d_norms = jnp.concatenate([d_pre_mix, d_post_mix, d_pre_mlp, d_post_mlp], axis=0)
    return loss_row[0, 0], grad_x, (d_norms, d_conv, sums, d_fox_norm, d_gdn_norm), reduced_late, reduced_in


def kernel(x, pre_mix_norm, w_in, fox_f_bias, fox_out_norm, gdn_conv_w, gdn_a_log, gdn_dt_bias, gdn_out_norm, w_out, post_mix_norm, pre_mlp_norm, w_up, w_down, post_mlp_norm, loss_target, m_pre_mix_norm, m_w_in, m_fox_f_bias, m_fox_out_norm, m_gdn_conv_w, m_gdn_a_log, m_gdn_dt_bias, m_gdn_out_norm, m_w_out, m_post_mix_norm, m_pre_mlp_norm, m_w_up, m_w_down, m_post_mlp_norm, v_pre_mix_norm, v_w_in, v_fox_f_bias, v_fox_out_norm, v_gdn_conv_w, v_gdn_a_log, v_gdn_dt_bias, v_gdn_out_norm, v_w_out, v_post_mix_norm, v_pre_mlp_norm, v_w_up, v_w_down, v_post_mlp_norm):
    weights = dict(pre_mix_norm=pre_mix_norm, w_in=w_in, fox_f_bias=fox_f_bias, fox_out_norm=fox_out_norm, gdn_conv_w=gdn_conv_w,
                   gdn_a_log=gdn_a_log, gdn_dt_bias=gdn_dt_bias, gdn_out_norm=gdn_out_norm, w_out=w_out, post_mix_norm=post_mix_norm,
                   pre_mlp_norm=pre_mlp_norm, w_up=w_up, w_down=w_down, post_mlp_norm=post_mlp_norm)
    m_in = dict(pre_mix_norm=m_pre_mix_norm, w_in=m_w_in, fox_f_bias=m_fox_f_bias, fox_out_norm=m_fox_out_norm, gdn_conv_w=m_gdn_conv_w,
                gdn_a_log=m_gdn_a_log, gdn_dt_bias=m_gdn_dt_bias, gdn_out_norm=m_gdn_out_norm, w_out=m_w_out, post_mix_norm=m_post_mix_norm,
                pre_mlp_norm=m_pre_mlp_norm, w_up=m_w_up, w_down=m_w_down, post_mlp_norm=m_post_mlp_norm)
    v_in = dict(pre_mix_norm=v_pre_mix_norm, w_in=v_w_in, fox_f_bias=v_fox_f_bias, fox_out_norm=v_fox_out_norm, gdn_conv_w=v_gdn_conv_w,
                gdn_a_log=v_gdn_a_log, gdn_dt_bias=v_gdn_dt_bias, gdn_out_norm=v_gdn_out_norm, w_out=v_w_out, post_mix_norm=v_post_mix_norm,
                pre_mlp_norm=v_pre_mlp_norm, w_up=v_w_up, w_down=v_w_down, post_mlp_norm=v_post_mlp_norm)
    order_w = ("pre_mix_norm", "w_in", "fox_f_bias", "fox_out_norm", "gdn_conv_w", "gdn_a_log", "gdn_dt_bias", "gdn_out_norm", "w_out",
               "post_mix_norm", "pre_mlp_norm", "w_up", "w_down", "post_mlp_norm")
    big = ("w_in", "w_out", "w_up", "w_down")

    def row(v):
        return v if v.ndim == 2 else v.reshape(1, -1)

    win_shard = jnp.pad(w_in.T.astype(BF16), ((0, D - CW), (0, 0)))
    win_g, conv_g = _run_exchange(_allgather_exchange([win_shard], whole=[gdn_conv_w]), "weights_allgather_in")
    win_p = _to_padded_rows(_with_own(win_g, win_shard))
    conv_full = _with_own(conv_g, gdn_conv_w).transpose(1, 0, 2).reshape(CONV_K, 3 * DGDN)
    late_shards = [weights[n].astype(BF16) for n in big[1:]]

    def resolve_late(gathered_a, gathered_b):
        wout_g, wup3, wdown_g = [_with_own(g, own) for g, own in zip(list(gathered_a) + list(gathered_b), late_shards)]
        return wout_g.reshape(D, D), wup3, wdown_g.reshape(DFF, D)

    pair_sums = {}

    def pair_reduced(names, blocks):
        theirs = _run_exchange(_pair_exchange(blocks), "grads_pair_exchange_" + names[0])
        for n, s in zip(names, _pair_sum(blocks, theirs, "grads_pair_sum_" + names[0])):
            pair_sums[n] = s
        return _chip_exchange([pair_sums[n] for n in names])

    def reduce_late(dwout, dwup3, dwdown):
        return pair_reduced(big[1:], [dwout.reshape(NCHIP, D // NCHIP, D), dwup3, dwdown.reshape(NCHIP, DFF // NCHIP, D)])

    def reduce_in(dwin_p):
        return pair_reduced(big[:1], [_from_padded_rows(dwin_p)])

    loss_local, grad_x, small, received_late, received_in = _local_step(
        x[0], loss_target[0], win_p, (_allgather_exchange(late_shards[:2]), _allgather_exchange(late_shards[2:]), resolve_late),
        reduce_late, reduce_in, row(pre_mix_norm), fox_f_bias, row(fox_out_norm), conv_full, gdn_a_log, gdn_dt_bias,
        row(gdn_out_norm), row(post_mix_norm), row(pre_mlp_norm), row(post_mlp_norm))
    loss = lax.psum(loss_local, ("x", "y", "c"))

    g_mine = _chip_sum([pair_sums[n] for n in big], list(received_in) + list(received_late))
    g_theirs = _run_exchange(_pair_share(g_mine), "grads_pair_share")

    g_small, d_small, nm_small, nv_small = _small_adamw(
        _small_allreduce(*small), [row(weights[n]) for n in SMALL_NAMES], [row(m_in[n]) for n in SMALL_NAMES],
        [row(v_in[n]) for n in SMALL_NAMES])

    g_big, d_big, nm_big, nv_big = _adamw_big([weights[n] for n in big[1:]], g_mine[1:], g_theirs[1:],
                                              [m_in[n] for n in big[1:]], [v_in[n] for n in big[1:]])
    in_t = _adamw_in(w_in.T, g_mine[0], g_theirs[0], m_w_in.T, v_w_in.T)

    grads, delta, new_m, new_v = {}, {}, {}, {}
    grads["w_in"], delta["w_in"], new_m["w_in"], new_v["w_in"] = [t.T for t in in_t]
    for i, n in enumerate(big[1:]):
        grads[n], delta[n], new_m[n], new_v[n] = g_big[i], d_big[i], nm_big[i], nv_big[i]
    for i, n in enumerate(SMALL_NAMES):
        shape = weights[n].shape
        grads[n], delta[n], new_m[n], new_v[n] = (g_small[i].reshape(shape), d_small[i].reshape(shape),
                                                  nm_small[i].reshape(shape), nv_small[i].reshape(shape))
    return (loss, grad_x[None], *[grads[n] for n in order_w], *[delta[n] for n in order_w], *[new_m[n] for n in order_w],
            *[new_v[n] for n in order_w])
```

```python
import jax
import jax.numpy as jnp
from jax import lax
from jax.experimental import pallas as pl
from jax.experimental.pallas import tpu as pltpu

F32 = jnp.float32
BF16 = jnp.bfloat16
MESH = pl.DeviceIdType.MESH

S = 2048
D = 1024
NFH, FHD = 8, 64
NPAIR = NFH // 2
NGH, GHD = 4, 128
DFOX = NFH * FHD
DGDN = NGH * GHD
CHUNK = 64
NCH = S // CHUNK
CONV_K = 4
DFF = 4 * D
EPS = 1e-6
DPROJ = 3600
LANES = 128
DPROJ_PAD = 3840
BLK_GDN = 12
BLK_GZ = 24
BLK_SMALL = 28
NCHIP = 4
NDEV = 8
VMEM_LIMIT = 56 * 1024 * 1024

ADAM_LR = 0.001
ADAM_B1 = 0.9
ADAM_B2 = 0.999
ADAM_EPS = 1e-08
ADAM_WD = 0.01
ADAM_STEP = 10


def _cparams(**kw):
    return pltpu.CompilerParams(vmem_limit_bytes=VMEM_LIMIT, **kw)


def _dn(ca, cb):
    return (((ca,), (cb,)), ((), ()))


def _dot(a, b, ca=1, cb=0):
    return lax.dot_general(a.astype(BF16), b.astype(BF16), _dn(ca, cb), preferred_element_type=F32)


def _hdot(a, b, ca=1, cb=0):
    return lax.dot_general(a.astype(F32), b.astype(F32), _dn(ca, cb), precision=lax.Precision.HIGHEST,
                           preferred_element_type=F32)


@jax.custom_vjp
def _mm_nn(a, b):
    return _dot(a, b, 1, 0)


def _mm_nn_fwd(a, b):
    return _dot(a, b, 1, 0), (a, b)


def _mm_nn_bwd(res, g):
    a, b = res
    return _dot(g, b, 1, 1), _dot(a, g, 0, 0)


_mm_nn.defvjp(_mm_nn_fwd, _mm_nn_bwd)


@jax.custom_vjp
def _mm_nt(a, b):
    return _dot(a, b, 1, 1)


def _mm_nt_fwd(a, b):
    return _dot(a, b, 1, 1), (a, b)


def _mm_nt_bwd(res, g):
    a, b = res
    return _dot(g, b, 1, 0), _dot(g, a, 0, 0)


_mm_nt.defvjp(_mm_nt_fwd, _mm_nt_bwd)


@jax.custom_vjp
def _saved_inverse(m, t_inv):
    del m
    return t_inv


def _saved_inverse_fwd(m, t_inv):
    del m
    return t_inv, t_inv


def _saved_inverse_bwd(t_inv, g):
    return -_hdot(_hdot(t_inv, g, 0, 0), t_inv, 1, 1), jnp.zeros_like(t_inv)


_saved_inverse.defvjp(_saved_inverse_fwd, _saved_inverse_bwd)


def _sigmoid(z):
    return 1.0 / (1.0 + jnp.exp(-z))


def _softplus(z):
    return jnp.maximum(z, 0.0) + jnp.log(1.0 + jnp.exp(-jnp.abs(z)))


def _silu(z):
    return z * _sigmoid(z)


def _rms_scale(x):
    return lax.rsqrt(jnp.mean(x * x, axis=-1, keepdims=True) + EPS)


def _rms_bwd(x, w, g):
    r = _rms_scale(x)
    gw = g * w
    dx = r * gw - x * (r * r * r) * jnp.mean(gw * x, axis=-1, keepdims=True)
    return dx, g * x * r


def _matmul(a, b, *, name, ta=False, tb=False, tm=512, tn=512, tk=512, out_dtypes=(F32,), b3=False, o3=False,
            extra=(), epilogue=None, exchange=None):
    m, k = (a.shape[1], a.shape[0]) if ta else a.shape
    if b3:
        n = b.shape[1] if tb else b.shape[0] * b.shape[2]
        kb = b.shape[0] * b.shape[2] if tb else b.shape[1]
    else:
        n, kb = (b.shape[0], b.shape[1]) if tb else (b.shape[1], b.shape[0])
    assert kb == k, (name, kb, k)
    tm, tn, tk = min(tm, m), min(tn, n), min(tk, k)
    assert m % tm == 0 and n % tn == 0 and k % tk == 0, (name, m, n, k, tm, tn, tk)
    nk = k // tk
    n_extra = len(extra)
    n_out = len(out_dtypes)
    grid = (m // tm, n // tn, nk)
    ex_in, ex_in_specs, ex_out_specs, ex_out_shape, ex_scratch = _hosted(exchange)

    def body(*refs):
        a_ref, b_ref = refs[0], refs[1]
        extra_refs = refs[2:2 + n_extra]
        first_out = 2 + n_extra + len(ex_in)
        out_refs = refs[first_out:first_out + n_out]
        ex_refs = refs[2 + n_extra:first_out] + refs[first_out + n_out:first_out + n_out + len(ex_out_shape)] + refs[-2:]
        step = [pl.program_id(d) for d in range(3)]

        if exchange is not None:
            @pl.when((step[0] == 0) & (step[1] == 0) & (step[2] == 0))
            def _():
                exchange.start(*exchange.split(ex_refs))

        def finish(acc):
            outs = (acc,) if epilogue is None else epilogue(acc, *[r[...] for r in extra_refs])
            for o_ref, val in zip(out_refs, outs):
                o_ref[...] = val.astype(o_ref.dtype)

        part = _dot(a_ref[...], b_ref[...], 0 if ta else 1, 1 if tb else 0)
        if nk == 1:
            finish(part)
        else:
            acc_ref = refs[first_out + n_out + len(ex_out_shape)]

            @pl.when(step[2] == 0)
            def _():
                acc_ref[...] = part

            @pl.when(step[2] > 0)
            def _():
                acc_ref[...] += part

            @pl.when(step[2] == nk - 1)
            def _():
                finish(acc_ref[...])

        if exchange is not None:
            @pl.when((step[0] == grid[0] - 1) & (step[1] == grid[1] - 1) & (step[2] == nk - 1))
            def _():
                exchange.finish(*exchange.split(ex_refs))

    a_spec = pl.BlockSpec((tk, tm), lambda i, j, kk: (kk, i)) if ta else pl.BlockSpec((tm, tk), lambda i, j, kk: (i, kk))
    if b3 and tb:
        assert b.shape[2] == tk
        b_spec = pl.BlockSpec((None, tn, tk), lambda i, j, kk: (kk, j, 0))
    elif b3:
        assert b.shape[2] == tn
        b_spec = pl.BlockSpec((None, tk, tn), lambda i, j, kk: (j, kk, 0))
    elif tb:
        b_spec = pl.BlockSpec((tn, tk), lambda i, j, kk: (j, kk))
    else:
        b_spec = pl.BlockSpec((tk, tn), lambda i, j, kk: (kk, j))
    tile = pl.BlockSpec((tm, tn), lambda i, j, kk: (i, j))
    out_specs = [tile] * n_out
    out_shape = [jax.ShapeDtypeStruct((m, n), dt) for dt in out_dtypes]
    if o3:
        out_specs[0] = pl.BlockSpec((None, tm, tn), lambda i, j, kk: (j, i, 0))
        out_shape[0] = jax.ShapeDtypeStruct((n // tn, m, tn), out_dtypes[0])
    res = pl.pallas_call(
        body, name=name, grid=grid,
        in_specs=[a_spec, b_spec] + [tile] * n_extra + ex_in_specs, out_specs=out_specs + ex_out_specs,
        out_shape=out_shape + ex_out_shape,
        scratch_shapes=([pltpu.VMEM((tm, tn), F32)] if nk > 1 else []) + ex_scratch,
        compiler_params=_cparams(),
    )(a, b, *extra, *ex_in)
    if exchange is not None:
        return (res[0] if n_out == 1 else res[:n_out]), res[n_out:]
    return res[0] if n_out == 1 else res


TR = 256


def _row_spec(cols):
    return pl.BlockSpec((TR, cols), lambda i: (i, 0))


def _vec_spec(cols):
    return pl.BlockSpec((1, cols), lambda i: (0, 0))


def _pre_norm(x, w):
    def body(x_ref, w_ref, h_ref):
        xv = x_ref[...]
        h_ref[...] = (xv * _rms_scale(xv) * w_ref[...]).astype(BF16)

    return pl.pallas_call(
        body, name="pre_norm", grid=(S // TR,), in_specs=[_row_spec(D), _vec_spec(D)], out_specs=_row_spec(D),
        out_shape=jax.ShapeDtypeStruct((S, D), BF16), compiler_params=_cparams(),
    )(x, w)


def _post_mix(x, mixed, w_post, w_pre_mlp):
    def body(x_ref, m_ref, wp_ref, wm_ref, x1_ref, h2_ref):
        mv = m_ref[...]
        x1 = x_ref[...] + mv * _rms_scale(mv) * wp_ref[...]
        x1_ref[...] = x1
        h2_ref[...] = (x1 * _rms_scale(x1) * wm_ref[...]).astype(BF16)

    return pl.pallas_call(
        body, name="post_mix", grid=(S // TR,),
        in_specs=[_row_spec(D), _row_spec(D), _vec_spec(D), _vec_spec(D)], out_specs=[_row_spec(D), _row_spec(D)],
        out_shape=[jax.ShapeDtypeStruct((S, D), F32), jax.ShapeDtypeStruct((S, D), BF16)], compiler_params=_cparams(),
    )(x, mixed, w_post, w_pre_mlp)


def _loss_head(x1, y, w_post_mlp, target):
    def body(x1_ref, y_ref, w_ref, t_ref, dx2_ref, dy_ref, dw_ref, loss_ref):
        i = pl.program_id(0)
        yv = y_ref[...]
        w = w_ref[...]
        x2 = x1_ref[...] + yv * _rms_scale(yv) * w
        err = x2 - t_ref[...]
        dx2 = err * (1.0 / D)
        dx2_ref[...] = dx2
        dy, dwt = _rms_bwd(yv, w, dx2)
        dy_ref[...] = dy.astype(BF16)

        @pl.when(i == 0)
        def _():
            dw_ref[...] = jnp.zeros_like(dw_ref)
            loss_ref[...] = jnp.zeros_like(loss_ref)

        dw_ref[...] += jnp.sum(dwt, axis=0, keepdims=True)
        part = 0.5 * jnp.sum(jnp.mean(err * err, axis=-1, keepdims=True), axis=0, keepdims=True)
        loss_ref[...] += jnp.broadcast_to(part, loss_ref.shape)

    return pl.pallas_call(
        body, name="loss_head", grid=(S // TR,),
        in_specs=[_row_spec(D), _row_spec(D), _vec_spec(D), _row_spec(D)],
        out_specs=[_row_spec(D), _row_spec(D), _vec_spec(D), _vec_spec(LANES)],
        out_shape=[jax.ShapeDtypeStruct((S, D), F32), jax.ShapeDtypeStruct((S, D), BF16),
                   jax.ShapeDtypeStruct((1, D), F32), jax.ShapeDtypeStruct((1, LANES), F32)],
        compiler_params=_cparams(),
    )(x1, y, w_post_mlp, target)


def _mid_bwd(dh2, x1, w_pre_mlp, dx2, mixed, w_post):
    def body(dh2_ref, x1_ref, wm_ref, dx2_ref, m_ref, wp_ref, dx1_ref, dm_ref, dwm_ref, dwp_ref):
        i = pl.program_id(0)
        dxa, dwm = _rms_bwd(x1_ref[...], wm_ref[...], dh2_ref[...])
        dx1 = dx2_ref[...] + dxa
        dx1_ref[...] = dx1
        dm, dwp = _rms_bwd(m_ref[...], wp_ref[...], dx1)
        dm_ref[...] = dm.astype(BF16)

        @pl.when(i == 0)
        def _():
            dwm_ref[...] = jnp.zeros_like(dwm_ref)
            dwp_ref[...] = jnp.zeros_like(dwp_ref)

        dwm_ref[...] += jnp.sum(dwm, axis=0, keepdims=True)
        dwp_ref[...] += jnp.sum(dwp, axis=0, keepdims=True)

    return pl.pallas_call(
        body, name="mid_bwd", grid=(S // TR,),
        in_specs=[_row_spec(D), _row_spec(D), _vec_spec(D), _row_spec(D), _row_spec(D), _vec_spec(D)],
        out_specs=[_row_spec(D), _row_spec(D), _vec_spec(D), _vec_spec(D)],
        out_shape=[jax.ShapeDtypeStruct((S, D), F32), jax.ShapeDtypeStruct((S, D), BF16),
                   jax.ShapeDtypeStruct((1, D), F32), jax.ShapeDtypeStruct((1, D), F32)],
        compiler_params=_cparams(),
    )(dh2, x1, w_pre_mlp, dx2, mixed, w_post)


def _pre_norm_bwd(dh, x, w, dx1):
    def body(dh_ref, x_ref, w_ref, dx1_ref, dx_ref, dw_ref):
        i = pl.program_id(0)
        dxa, dwt = _rms_bwd(x_ref[...], w_ref[...], dh_ref[...])
        dx_ref[...] = dx1_ref[...] + dxa

        @pl.when(i == 0)
        def _():
            dw_ref[...] = jnp.zeros_like(dw_ref)

        dw_ref[...] += jnp.sum(dwt, axis=0, keepdims=True)

    return pl.pallas_call(
        body, name="pre_norm_bwd", grid=(S // TR,),
        in_specs=[_row_spec(D), _row_spec(D), _vec_spec(D), _row_spec(D)], out_specs=[_row_spec(D), _vec_spec(D)],
        out_shape=[jax.ShapeDtypeStruct((S, D), F32), jax.ShapeDtypeStruct((1, D), F32)], compiler_params=_cparams(),
    )(dh, x, w, dx1)


BQ = 256
NQ = S // BQ
LANE_BETA, LANE_G = 8, 12


def _gate_lanes(shape):
    lane = lax.broadcasted_iota(jnp.int32, shape, 1)
    return lane < LANE_BETA, (lane >= LANE_BETA) & (lane < LANE_G), (lane >= LANE_G) & (lane < LANE_G + NGH)


def _gates(proj, bias_vec, alog_vec):
    def body(s_ref, b_ref, a_ref, o_ref, carry_ref):
        i = pl.program_id(0)

        @pl.when(i == 0)
        def _():
            carry_ref[...] = jnp.zeros_like(carry_ref)

        z = s_ref[...] + b_ref[...]
        tail = jnp.log(1.0 + jnp.exp(-jnp.abs(z)))
        sp = jnp.maximum(z, 0.0) + tail
        lf = jnp.minimum(z, 0.0) - tail
        r = lax.broadcasted_iota(jnp.int32, (BQ, BQ), 0)
        c = lax.broadcasted_iota(jnp.int32, (BQ, BQ), 1)
        tri = (c <= r).astype(F32)
        cum = _hdot(tri, lf) + carry_ref[...]
        carry_ref[...] = cum[BQ - 1:BQ, :]
        is_fox, is_beta, is_g = _gate_lanes(z.shape)
        o_ref[...] = jnp.where(is_fox, cum, jnp.where(is_beta, _sigmoid(z), jnp.where(is_g, -jnp.exp(a_ref[...]) * sp, 0.0)))

    return pl.pallas_call(
        body, name="gates", grid=(NQ,),
        in_specs=[pl.BlockSpec((BQ, LANES), lambda i: (i, BLK_SMALL)), _vec_spec(LANES), _vec_spec(LANES)],
        out_specs=pl.BlockSpec((BQ, LANES), lambda i: (i, 0)), out_shape=jax.ShapeDtypeStruct((S, LANES), F32),
        scratch_shapes=[pltpu.VMEM((1, LANES), F32)], compiler_params=_cparams(),
    )(proj, bias_vec, alog_vec)


def _gates_bwd(proj, bias_vec, alog_vec, dgates_gdn, dcum_fox, dproj):
    def body(s_ref, b_ref, a_ref, dg_ref, dc_ref, dproj_in, dproj_ref, red_ref, carry_ref):
        del dproj_in
        i = pl.program_id(0)

        @pl.when(i == 0)
        def _():
            carry_ref[...] = jnp.zeros_like(carry_ref)
            red_ref[...] = jnp.zeros_like(red_ref)

        z = s_ref[...] + b_ref[...]
        dg = dg_ref[...] + dc_ref[...]
        r = lax.broadcasted_iota(jnp.int32, (BQ, BQ), 0)
        c = lax.broadcasted_iota(jnp.int32, (BQ, BQ), 1)
        upper = (c >= r).astype(F32)
        dlf = _hdot(upper, dg) + carry_ref[...]
        carry_ref[...] = dlf[0:1, :]
        sig = _sigmoid(z)
        g_scale = -jnp.exp(a_ref[...])
        is_fox, is_beta, is_g = _gate_lanes(z.shape)
        ds = jnp.where(is_fox, dlf * (1.0 - sig), jnp.where(is_beta, dg * sig * (1.0 - sig), jnp.where(is_g, dg * g_scale * sig, 0.0)))
        dproj_ref[:, 0:LANES] = ds.astype(BF16)
        dproj_ref[:, LANES:2 * LANES] = jnp.zeros((BQ, LANES), BF16)
        dalog = jnp.where(is_g, dg * g_scale * _softplus(z), 0.0)
        sums = jnp.sum(ds, axis=0, keepdims=True)
        red_ref[0:1, :] += jnp.where(is_fox[0:1], sums, 0.0)
        red_ref[1:2, :] += pltpu.roll(jnp.where(is_g[0:1], sums, 0.0), LANES - LANE_G, 1)
        red_ref[2:3, :] += pltpu.roll(jnp.sum(dalog, axis=0, keepdims=True), LANES - LANE_G, 1)

    blk = pl.BlockSpec((BQ, LANES), lambda i: (NQ - 1 - i, 0))
    return pl.pallas_call(
        body, name="gates_bwd", grid=(NQ,),
        in_specs=[pl.BlockSpec((BQ, LANES), lambda i: (NQ - 1 - i, BLK_SMALL)), _vec_spec(LANES), _vec_spec(LANES), blk, blk,
                  pl.BlockSpec(memory_space=pl.ANY)],
        out_specs=[pl.BlockSpec((BQ, 2 * LANES), lambda i: (NQ - 1 - i, BLK_SMALL // 2)), pl.BlockSpec((8, LANES), lambda i: (0, 0))],
        out_shape=[jax.ShapeDtypeStruct((S, DPROJ_PAD), BF16), jax.ShapeDtypeStruct((8, LANES), F32)],
        input_output_aliases={5: 0},
        scratch_shapes=[pltpu.VMEM((1, LANES), F32)], compiler_params=_cparams(),
    )(proj, bias_vec, alog_vec, dgates_gdn, dcum_fox, dproj)


FOX_SCALE = FHD ** -0.5
FOX_PAIRS = 2
FOX_PAIRS_BWD = 2


def _head_mask(e):
    lane = lax.broadcasted_iota(jnp.int32, (1, LANES), 1)
    return (lane >= e * FHD) & (lane < (e + 1) * FHD)


def _lane_col(vals, index):
    lane = lax.broadcasted_iota(jnp.int32, vals.shape, 1)
    return jnp.sum(jnp.where(lane == index, vals, 0.0), axis=1, keepdims=True)


def _sublane_row(vals, index):
    row = lax.broadcasted_iota(jnp.int32, vals.shape, 0)
    return jnp.sum(jnp.where(row == index, vals, 0.0), axis=0, keepdims=True)


def _pair_cols(c0, c1):
    lane = lax.broadcasted_iota(jnp.int32, (c0.shape[0], 2), 1)
    return jnp.where(lane == 0, c0, c1)


def _split3(x):
    hi = x.astype(BF16).astype(F32)
    rest = x - hi
    mid = rest.astype(BF16).astype(F32)
    return hi, mid, (rest - mid).astype(BF16).astype(F32)


def _fox_operand(vals, e, cum, is_query):
    lane = lax.broadcasted_iota(jnp.int32, (1, LANES), 1)
    base = (1 - e) * FHD
    parts = _split3(cum)
    own = jnp.where(_head_mask(e), vals * FOX_SCALE if is_query else vals, 0.0)
    cum_at, ones_at = (base, base + 3) if is_query else (base + 3, base)
    sign = 1.0 if is_query else -1.0
    out = own + jnp.where((lane >= ones_at) & (lane < ones_at + 3), 1.0, 0.0)
    for i, part in enumerate(parts):
        out = out + jnp.where(lane == cum_at + i, sign * part, 0.0)
    return out.astype(BF16)


def _causal_block():
    return lax.broadcasted_iota(jnp.int32, (BQ, BQ), 1) <= lax.broadcasted_iota(jnp.int32, (BQ, BQ), 0)


def _head_rms(o, masks):
    o2 = o * o
    r = [lax.rsqrt(jnp.sum(jnp.where(mk, o2, 0.0), axis=1, keepdims=True) * (1.0 / FHD) + EPS) for mk in masks]
    return jnp.where(masks[0], r[0], r[1])


def _hosted(exchange):
    if exchange is None:
        return [], [], [], [], []
    return (exchange.inputs, [HBM] * len(exchange.inputs), [HBM] * len(exchange.out_shape), exchange.out_shape,
            exchange.sem_shapes())


def _fox_fwd(proj, gates, w2, exchange=None):
    ex_in, ex_in_specs, ex_out_specs, ex_out_shape, ex_scratch = _hosted(exchange)

    n_in = 3 * FOX_PAIRS + 2
    heads = [(pp, e) for pp in range(FOX_PAIRS) for e in range(2)]

    def body(*refs):
        qkv_refs, g_ref, w_ref = refs[:3 * FOX_PAIRS], refs[3 * FOX_PAIRS], refs[3 * FOX_PAIRS + 1]
        mix_ref, o_ref, lse_ref = refs[n_in + len(ex_in):n_in + 3 + len(ex_in)]
        ka_ref, vb_ref = refs[n_in + 3 + len(ex_in) + len(ex_out_shape):n_in + 5 + len(ex_in) + len(ex_out_shape)]
        ex_refs = refs[n_in:n_in + len(ex_in)] + refs[n_in + 3 + len(ex_in):n_in + 3 + len(ex_in) + len(ex_out_shape)] + refs[-2:]
        grp, qi = pl.program_id(0), pl.program_id(1)

        def head_index(pp, e):
            return 2 * (FOX_PAIRS * grp + pp) + e

        if exchange is not None:
            @pl.when((grp == 0) & (qi == 0))
            def _():
                exchange.start(*exchange.split(ex_refs))

        @pl.when(qi == 0)
        def _():
            gt = g_ref[...]
            for pp in range(FOX_PAIRS):
                kv = qkv_refs[3 * pp + 1][...]
                for e in range(2):
                    ka_ref[2 * pp + e] = _fox_operand(kv, e, _lane_col(gt, head_index(pp, e)), False)
                vb_ref[pp] = qkv_refs[3 * pp + 2][...].astype(BF16)

        masks = [_head_mask(0), _head_mask(1)]
        gt = g_ref[pl.ds(pl.multiple_of(qi * BQ, BQ), BQ), :]
        qs = [_fox_operand(qkv_refs[3 * pp][...], e, _lane_col(gt, head_index(pp, e)), True) for pp, e in heads]
        n = range(len(heads))

        def block(kj, carry, diagonal):
            rows = pl.ds(pl.multiple_of(kj * BQ, BQ), BQ)
            s = [_dot(qs[i], ka_ref[i, rows, :], 1, 1) for i in n]
            if diagonal:
                s = [jnp.where(_causal_block(), s[i], -jnp.inf) for i in n]
            m_new = [jnp.maximum(carry[i][0], jnp.max(s[i], axis=-1, keepdims=True)) for i in n]
            p = [jnp.exp(s[i] - m_new[i]) for i in n]
            alpha = [jnp.exp(carry[i][0] - m_new[i]) for i in n]
            l_new = [alpha[i] * carry[i][1] + jnp.sum(p[i], axis=-1, keepdims=True) for i in n]
            pv = [_dot(p[i], vb_ref[heads[i][0], rows, :]) for i in n]
            return tuple((m_new[i], l_new[i], alpha[i] * carry[i][2] + pv[i]) for i in n)

        one = (jnp.full((BQ, 1), -jnp.inf, F32), jnp.zeros((BQ, 1), F32), jnp.zeros((BQ, LANES), F32))
        below = lax.fori_loop(0, qi, lambda kj, carry: block(kj, carry, False), (one,) * len(heads))
        done = block(qi, below, True)
        for pp in range(FOX_PAIRS):
            (m0, l0, a0), (m1, l1, a1) = done[2 * pp], done[2 * pp + 1]
            o = jnp.where(masks[0], a0 / l0, a1 / l1)
            cols = slice(pp * LANES, (pp + 1) * LANES)
            o_ref[:, cols] = o
            mix_ref[:, cols] = (o * _head_rms(o, masks) * w_ref[...]).astype(BF16)
            lse_ref[pp] = _pair_cols(m0 + jnp.log(l0), m1 + jnp.log(l1))

        if exchange is not None:
            @pl.when((grp == NPAIR // FOX_PAIRS - 1) & (qi == NQ - 1))
            def _():
                exchange.finish(*exchange.split(ex_refs))

    qkv_specs = []
    for pp in range(FOX_PAIRS):
        qkv_specs.append(pl.BlockSpec((BQ, LANES), lambda g, i, pp=pp: (i, 3 * (FOX_PAIRS * g + pp))))
        qkv_specs.append(pl.BlockSpec((S, LANES), lambda g, i, pp=pp: (0, 3 * (FOX_PAIRS * g + pp) + 1)))
        qkv_specs.append(pl.BlockSpec((S, LANES), lambda g, i, pp=pp: (0, 3 * (FOX_PAIRS * g + pp) + 2)))
    blk = pl.BlockSpec((BQ, FOX_PAIRS * LANES), lambda g, i: (i, g))
    res = pl.pallas_call(
        body, name="fox_fwd", grid=(NPAIR // FOX_PAIRS, NQ),
        in_specs=qkv_specs + [pl.BlockSpec((S, LANES), lambda g, i: (0, 0)), pl.BlockSpec((1, LANES), lambda g, i: (0, 0))]
        + ex_in_specs,
        out_specs=[blk, blk, pl.BlockSpec((FOX_PAIRS, BQ, 2), lambda g, i: (g, i, 0))] + ex_out_specs,
        out_shape=[jax.ShapeDtypeStruct((S, D), BF16), jax.ShapeDtypeStruct((S, DFOX), F32),
                   jax.ShapeDtypeStruct((NPAIR, S, 2), F32)] + ex_out_shape,
        scratch_shapes=[pltpu.VMEM((2 * FOX_PAIRS, S, LANES), BF16), pltpu.VMEM((FOX_PAIRS, S, LANES), BF16)] + ex_scratch,
        compiler_params=_cparams(),
    )(*([proj] * (3 * FOX_PAIRS)), gates, w2, *ex_in)
    return res[0], res[1], res[2], res[3:]


def _fox_norm_bwd(o, dmix, w2):
    def body(o_ref, g_ref, w_ref, do_ref, dl_ref, dw_ref):
        hp, qi = pl.program_id(0), pl.program_id(1)
        masks = [_head_mask(0), _head_mask(1)]
        ov = o_ref[...]
        g = g_ref[...]
        r = _head_rms(ov, masks)
        gw = g * w_ref[...]
        gwo = gw * ov
        mean = [jnp.sum(jnp.where(mk, gwo, 0.0), axis=1, keepdims=True) * (1.0 / FHD) for mk in masks]
        do = r * gw - ov * (r * r * r) * jnp.where(masks[0], mean[0], mean[1])
        do_ref[...] = do.astype(BF16)
        doo = do * ov
        dl_ref[...] = _pair_cols(*[jnp.sum(jnp.where(mk, doo, 0.0), axis=1, keepdims=True) for mk in masks])

        @pl.when((hp == 0) & (qi == 0))
        def _():
            dw_ref[...] = jnp.zeros_like(dw_ref)

        dw_ref[...] += jnp.sum(g * ov * r, axis=0, keepdims=True)

        @pl.when((hp == NPAIR - 1) & (qi == NQ - 1))
        def _():
            dw = dw_ref[...]
            dw_ref[...] = dw + pltpu.roll(dw, FHD, 1)

    blk = pl.BlockSpec((BQ, LANES), lambda hp, i: (i, hp))
    vec = pl.BlockSpec((1, LANES), lambda hp, i: (0, 0))
    return pl.pallas_call(
        body, name="fox_norm_bwd", grid=(NPAIR, NQ), in_specs=[blk, blk, vec],
        out_specs=[blk, pl.BlockSpec((None, BQ, 2), lambda hp, i: (hp, i, 0)), vec],
        out_shape=[jax.ShapeDtypeStruct((S, DFOX), BF16), jax.ShapeDtypeStruct((NPAIR, S, 2), F32),
                   jax.ShapeDtypeStruct((1, LANES), F32)],
        compiler_params=_cparams(),
    )(o, dmix, w2)


def _fox_bwd(proj, do, gates, lse, delta, exchange=None):
    ex_in, ex_in_specs, ex_out_specs, ex_out_shape, ex_scratch = _hosted(exchange)

    pg = FOX_PAIRS_BWD
    n_in = 3 * pg + 4
    heads = [(pp, e) for pp in range(pg) for e in range(2)]

    def body(*refs):
        qkv_refs = refs[:3 * pg]
        do_ref, g_ref, lse_ref, dl_ref = refs[3 * pg:n_in]
        dproj_ref, dc_ref = refs[n_in + len(ex_in):n_in + 2 + len(ex_in)]
        qa_ref, dq_ref = refs[n_in + 2 + len(ex_in) + len(ex_out_shape):n_in + 4 + len(ex_in) + len(ex_out_shape)]
        ex_refs = refs[n_in:n_in + len(ex_in)] + refs[n_in + 2 + len(ex_in):n_in + 2 + len(ex_in) + len(ex_out_shape)] + refs[-2:]
        grp, kj = pl.program_id(0), pl.program_id(1)

        def head_index(pp, e):
            return 2 * (pg * grp + pp) + e

        if exchange is not None:
            @pl.when((grp == 0) & (kj == 0))
            def _():
                exchange.start(*exchange.split(ex_refs))

        @pl.when(kj == 0)
        def _():
            gt = g_ref[...]
            for pp in range(pg):
                qv = qkv_refs[3 * pp][...]
                for e in range(2):
                    qa_ref[2 * pp + e] = _fox_operand(qv, e, _lane_col(gt, head_index(pp, e)), True)
            dq_ref[...] = jnp.zeros_like(dq_ref)

        @pl.when((grp == 0) & (kj == 0))
        def _():
            dc_ref[...] = jnp.zeros_like(dc_ref)

        masks = [_head_mask(0), _head_mask(1)]
        krows = pl.ds(pl.multiple_of(kj * BQ, BQ), BQ)
        gk = g_ref[krows, :]
        kas = [_fox_operand(qkv_refs[3 * pp + 1][...], e, _lane_col(gk, head_index(pp, e)), False) for pp, e in heads]
        vbs = [qkv_refs[3 * pp + 2][...].astype(BF16) for pp in range(pg)]
        lane = lax.broadcasted_iota(jnp.int32, (BQ, LANES), 1)
        n = range(len(heads))

        def block(qi, carry, diagonal):
            dks, dvs, css = carry
            rows = pl.ds(pl.multiple_of(qi * BQ, BQ), BQ)
            qa = [qa_ref[i, rows, :] for i in n]
            s = [_dot(qa[i], kas[i], 1, 1) for i in n]
            if diagonal:
                s = [jnp.where(_causal_block(), s[i], -jnp.inf) for i in n]
            dov = [do_ref[rows, pp * LANES:(pp + 1) * LANES] for pp in range(pg)]
            doe = [jnp.where(masks[e], dov[pp], jnp.zeros_like(dov[pp])) for pp, e in heads]
            lse2 = [lse_ref[pp, rows, :] for pp in range(pg)]
            dl2 = [dl_ref[pp, rows, :] for pp in range(pg)]
            p = [jnp.exp(s[i] - _lane_col(lse2[heads[i][0]], heads[i][1])) for i in n]
            dp = [_dot(doe[i], vbs[heads[i][0]], 1, 1) for i in n]
            ds = [p[i] * (dp[i] - _lane_col(dl2[heads[i][0]], heads[i][1])) for i in n]
            dv_part = [_dot(p[i], doe[i], 0, 0) for i in n]
            dk_part = [_dot(ds[i], jnp.where(masks[heads[i][1]], qa[i], jnp.zeros_like(qa[i])), 0, 0) for i in n]
            dq_part = [jnp.where(masks[heads[i][1]], _dot(ds[i], kas[i]), 0.0) for i in n]
            css = tuple(css[i] + jnp.sum(ds[i], axis=0, keepdims=True) for i in n)
            dc = jnp.zeros((BQ, LANES), F32)
            for i in n:
                dc = dc + jnp.where(lane == head_index(*heads[i]), jnp.sum(ds[i], axis=1, keepdims=True), 0.0)
            for pp in range(pg):
                dq_ref[pp, rows, :] += (dq_part[2 * pp] + dq_part[2 * pp + 1]) * FOX_SCALE
            dc_ref[rows, :] += dc
            dks = tuple(dks[pp] + dk_part[2 * pp] + dk_part[2 * pp + 1] for pp in range(pg))
            dvs = tuple(dvs[pp] + dv_part[2 * pp] + dv_part[2 * pp + 1] for pp in range(pg))
            return dks, dvs, css

        zero = jnp.zeros((BQ, LANES), F32)
        first = block(kj, ((zero,) * pg, (zero,) * pg, (jnp.zeros((1, BQ), F32),) * len(heads)), True)
        dks, dvs, css = lax.fori_loop(kj + 1, NQ, lambda qi, carry: block(qi, carry, False), first)
        r = lax.broadcasted_iota(jnp.int32, (BQ, BQ), 0)
        c = lax.broadcasted_iota(jnp.int32, (BQ, BQ), 1)
        dcol = jnp.zeros((BQ, LANES), F32)
        for i in n:
            col = jnp.sum(jnp.where(r == c, css[i], 0.0), axis=1, keepdims=True)
            dcol = dcol + jnp.where(lane == head_index(*heads[i]), col, 0.0)
        dc_ref[krows, :] -= dcol
        for pp in range(pg):
            base = 3 * pp * LANES
            dproj_ref[krows, base + LANES:base + 2 * LANES] = dks[pp].astype(BF16)
            dproj_ref[krows, base + 2 * LANES:base + 3 * LANES] = dvs[pp].astype(BF16)

        @pl.when(kj == NQ - 1)
        def _():
            for pp in range(pg):
                dproj_ref[:, 3 * pp * LANES:(3 * pp + 1) * LANES] = dq_ref[pp].astype(BF16)

        if exchange is not None:
            @pl.when((grp == NPAIR // pg - 1) & (kj == NQ - 1))
            def _():
                exchange.finish(*exchange.split(ex_refs))

    qkv_specs = []
    for pp in range(pg):
        qkv_specs.append(pl.BlockSpec((S, LANES), lambda g, j, pp=pp: (0, 3 * (pg * g + pp))))
        qkv_specs.append(pl.BlockSpec((BQ, LANES), lambda g, j, pp=pp: (j, 3 * (pg * g + pp) + 1)))
        qkv_specs.append(pl.BlockSpec((BQ, LANES), lambda g, j, pp=pp: (j, 3 * (pg * g + pp) + 2)))
    pair = pl.BlockSpec((pg, S, 2), lambda g, j: (g, 0, 0))
    res = pl.pallas_call(
        body, name="fox_bwd", grid=(NPAIR // pg, NQ),
        in_specs=qkv_specs + [pl.BlockSpec((S, pg * LANES), lambda g, j: (0, g)), pl.BlockSpec((S, LANES), lambda g, j: (0, 0)),
                              pair, pair] + ex_in_specs,
        out_specs=[pl.BlockSpec((S, 3 * pg * LANES), lambda g, j: (0, g)), pl.BlockSpec((S, LANES), lambda g, j: (0, 0))]
        + ex_out_specs,
        out_shape=[jax.ShapeDtypeStruct((S, DPROJ_PAD), BF16), jax.ShapeDtypeStruct((S, LANES), F32)] + ex_out_shape,
        scratch_shapes=[pltpu.VMEM((2 * pg, S, LANES), BF16), pltpu.VMEM((pg, S, LANES), F32)] + ex_scratch,
        compiler_params=_cparams(),
    )(*([proj] * (3 * pg)), do, gates, lse, delta, *ex_in)
    return res[0], res[1], res[2:]


NQKV = 3 * NGH
GDN_QSCALE = GHD ** -0.5


def _shift_down(x, s):
    if s == 0:
        return x
    row = lax.broadcasted_iota(jnp.int32, x.shape, 0)
    return jnp.where(row >= s, pltpu.roll(x, s, 0), 0.0)


def _shift_up(x, s):
    if s == 0:
        return x
    n = x.shape[0]
    row = lax.broadcasted_iota(jnp.int32, x.shape, 0)
    return jnp.where(row < n - s, pltpu.roll(x, n - s, 0), 0.0)


def _conv_pre(xv, wv):
    pre = xv * wv[CONV_K - 1:CONV_K, :]
    for j in range(CONV_K - 1):
        pre = pre + _shift_down(xv, CONV_K - 1 - j) * wv[j:j + 1, :]
    return pre


def _l2_factors(b):
    return b < 2 * NGH, jnp.where(b < NGH, GDN_QSCALE, 1.0)


def _gdn_pre(proj, conv_w):
    def body(x_ref, w_ref, o_ref):
        b = pl.program_id(0)
        c = _silu(_conv_pre(x_ref[...], w_ref[...]))
        normed, scale = _l2_factors(b)
        rs = lax.rsqrt(jnp.sum(c * c, axis=-1, keepdims=True) + EPS)
        o_ref[...] = c * jnp.where(normed, rs, 1.0) * scale

    return pl.pallas_call(
        body, name="gdn_pre", grid=(NQKV,),
        in_specs=[pl.BlockSpec((S, GHD), lambda b: (0, BLK_GDN + b)), pl.BlockSpec((CONV_K, GHD), lambda b: (0, b))],
        out_specs=pl.BlockSpec((S, GHD), lambda b: (0, b)),
        out_shape=jax.ShapeDtypeStruct((S, NQKV * GHD), F32), compiler_params=_cparams(),
    )(proj, conv_w)


def _gdn_pre_bwd(proj, conv_w, dqkv, dproj):
    def body(x_ref, w_ref, dy_ref, dproj_in, dx_ref, dw_ref):
        del dproj_in
        b = pl.program_id(0)
        xv = x_ref[...]
        wv = w_ref[...]
        pre = _conv_pre(xv, wv)
        sig = _sigmoid(pre)
        c = pre * sig
        normed, scale = _l2_factors(b)
        g = dy_ref[...] * scale
        rs = lax.rsqrt(jnp.sum(c * c, axis=-1, keepdims=True) + EPS)
        dc_n = rs * g - c * (rs * rs * rs) * jnp.sum(g * c, axis=-1, keepdims=True)
        dc = jnp.where(normed, dc_n, g)
        dpre = dc * sig * (1.0 + pre * (1.0 - sig))
        dx = dpre * wv[CONV_K - 1:CONV_K, :]
        for j in range(CONV_K - 1):
            dx = dx + _shift_up(dpre, CONV_K - 1 - j) * wv[j:j + 1, :]
        dx_ref[...] = dx.astype(BF16)
        for j in range(CONV_K):
            dw_ref[j:j + 1, :] = jnp.sum(dpre * _shift_down(xv, CONV_K - 1 - j), axis=0, keepdims=True)

    return pl.pallas_call(
        body, name="gdn_pre_bwd", grid=(NQKV,),
        in_specs=[pl.BlockSpec((S, GHD), lambda b: (0, BLK_GDN + b)), pl.BlockSpec((CONV_K, GHD), lambda b: (0, b)),
                  pl.BlockSpec((None, S, GHD), lambda b: (b // NGH, 0, b % NGH)), pl.BlockSpec(memory_space=pl.ANY)],
        out_specs=[pl.BlockSpec((S, GHD), lambda b: (0, BLK_GDN + b)), pl.BlockSpec((CONV_K, GHD), lambda b: (0, b))],
        out_shape=[jax.ShapeDtypeStruct((S, DPROJ_PAD), BF16), jax.ShapeDtypeStruct((CONV_K, NQKV * GHD), F32)],
        input_output_aliases={3: 0}, compiler_params=_cparams(),
    )(proj, conv_w, dqkv, dproj)


CB = 4
NCB = NCH // CB


def _chunk_prep(qs, ks, vs, gcols, bcols, t_saved=None):
    n = range(len(qs))
    r = lax.broadcasted_iota(jnp.int32, (CHUNK, CHUNK), 0)
    c = lax.broadcasted_iota(jnp.int32, (CHUNK, CHUNK), 1)
    incl = c <= r
    eye = (r == c).astype(F32)
    grow = [jnp.sum(gcols[i] * eye, axis=0, keepdims=True) for i in n]
    gc_col = [jnp.sum(jnp.where(incl, grow[i], 0.0), axis=1, keepdims=True) for i in n]
    gc_row = [jnp.sum(jnp.where(r <= c, gcols[i], 0.0), axis=0, keepdims=True) for i in n]
    decay = [jnp.exp(jnp.where(incl, gc_col[i] - gc_row[i], -jnp.inf)) for i in n]
    kb = [ks[i] * bcols[i] for i in n]
    vb = [vs[i] * bcols[i] for i in n]
    kk = [_mm_nt(kb[i], ks[i]) for i in n]
    m = [jnp.where(c < r, kk[i] * decay[i], 0.0) for i in n]
    if t_saved is None:
        t_inv = [eye - m[i] for i in n]
        p = [_hdot(m[i], m[i]) for i in n]
        for step in range(5):
            t_inv = [t_inv[i] + _hdot(t_inv[i], p[i]) for i in n]
            if step < 4:
                p = [_hdot(p[i], p[i]) for i in n]
    else:
        t_inv = [_saved_inverse(m[i], t_saved[i]) for i in n]
    egc = [jnp.exp(gc_col[i]) for i in n]
    u = [_mm_nn(t_inv[i], vb[i]) for i in n]
    w = [_mm_nn(t_inv[i], kb[i] * egc[i]) for i in n]
    qk = [_mm_nt(qs[i], ks[i]) for i in n]
    gc_last = [gc_col[i][CHUNK - 1:CHUNK, :] for i in n]
    return [(u[i], w[i], qk[i] * decay[i], qs[i] * egc[i], ks[i] * jnp.exp(gc_last[i] - gc_col[i]), jnp.exp(gc_last[i]),
             t_inv[i]) for i in n]


def _prep_specs():
    rows = CB * CHUNK
    qs = pl.BlockSpec((rows, GHD), lambda i, h: (i, h))
    ks = pl.BlockSpec((rows, GHD), lambda i, h: (i, NGH + h))
    vs = pl.BlockSpec((rows, GHD), lambda i, h: (i, 2 * NGH + h))
    gs = pl.BlockSpec((rows, LANES), lambda i, h: (i, 0))
    a_s = pl.BlockSpec((None, rows, CHUNK), lambda i, h: (h, i, 0))
    gl_s = pl.BlockSpec((None, CB, 1, LANES), lambda i, h: (h, i, 0, 0))
    return qs, ks, vs, gs, a_s, gl_s


def _gdn_prep(qkv, gates, exchange=None):
    ex_in, ex_in_specs, ex_out_specs, ex_out_shape, ex_scratch = _hosted(exchange)

    def body(*refs):
        q_ref, k_ref, v_ref, g_ref = refs[:4]
        u_ref, w_ref, qd_ref, kd_ref, a_ref, gl_ref, t_ref = refs[4 + len(ex_in):11 + len(ex_in)]
        ex_refs = refs[4:4 + len(ex_in)] + refs[11 + len(ex_in):]
        h = pl.program_id(1)

        if exchange is not None:
            @pl.when((pl.program_id(0) == 0) & (h == 0))
            def _():
                exchange.start(*exchange.split(ex_refs))

        chunks = [pl.ds(cidx * CHUNK, CHUNK) for cidx in range(CB)]
        gts = [g_ref[rows, :] for rows in chunks]
        outs = _chunk_prep([q_ref[rows, :] for rows in chunks], [k_ref[rows, :] for rows in chunks],
                           [v_ref[rows, :] for rows in chunks], [_lane_col(gt, LANE_G + h) for gt in gts],
                           [_lane_col(gt, LANE_BETA + h) for gt in gts])
        for cidx, rows in enumerate(chunks):
            u, w, a, qd, kd, gl, t_inv = outs[cidx]
            u_ref[rows, :] = u
            w_ref[rows, :] = w
            qd_ref[rows, :] = qd
            kd_ref[rows, :] = kd
            a_ref[rows, :] = a
            t_ref[rows, :] = t_inv
            gl_ref[cidx] = jnp.broadcast_to(gl, (1, LANES))

        if exchange is not None:
            @pl.when((pl.program_id(0) == NCB - 1) & (h == NGH - 1))
            def _():
                exchange.finish(*exchange.split(ex_refs))

    qs, ks, vs, gs, a_s, gl_s = _prep_specs()
    tok = jax.ShapeDtypeStruct((S, DGDN), F32)
    sq = jax.ShapeDtypeStruct((NGH, S, CHUNK), F32)
    res = pl.pallas_call(
        body, name="gdn_prep", grid=(NCB, NGH), in_specs=[qs, ks, vs, gs] + ex_in_specs,
        out_specs=[qs, qs, qs, qs, a_s, gl_s, a_s] + ex_out_specs,
        out_shape=[tok, tok, tok, tok, sq, jax.ShapeDtypeStruct((NGH, NCH, 1, LANES), F32), sq] + ex_out_shape,
        scratch_shapes=ex_scratch, compiler_params=_cparams(),
    )(qkv, qkv, qkv, gates, *ex_in)
    return res[:7], res[7:]


def _gdn_prep_bwd(qkv, gates, t_inv, du, dw, dqd, dkd, da, dgl):
    def body(q_ref, k_ref, v_ref, g_ref, t_ref, du_ref, dw_ref, dqd_ref, dkd_ref, da_ref, dgl_ref, dqkv_ref, dg_ref):
        h = pl.program_id(1)

        @pl.when(h == 0)
        def _():
            dg_ref[...] = jnp.zeros_like(dg_ref)

        lane = lax.broadcasted_iota(jnp.int32, (CHUNK, LANES), 1)
        chunks = [pl.ds(cidx * CHUNK, CHUNK) for cidx in range(CB)]
        gts = [g_ref[rows, :] for rows in chunks]
        t_saved = [t_ref[rows, :] for rows in chunks]
        _, vjp = jax.vjp(lambda *args: [o[:6] for o in _chunk_prep(*args, t_saved=t_saved)],
                         [q_ref[rows, :] for rows in chunks], [k_ref[rows, :] for rows in chunks],
                         [v_ref[rows, :] for rows in chunks], [_lane_col(gt, LANE_G + h) for gt in gts],
                         [_lane_col(gt, LANE_BETA + h) for gt in gts])
        dqs, dks, dvs, dgcs, dbcs = vjp([(du_ref[rows, :], dw_ref[rows, :], da_ref[rows, :], dqd_ref[rows, :],
                                          dkd_ref[rows, :], dgl_ref[cidx][:, 0:1]) for cidx, rows in enumerate(chunks)])
        for cidx, rows in enumerate(chunks):
            dq, dk, dv, dgc, dbc = dqs[cidx], dks[cidx], dvs[cidx], dgcs[cidx], dbcs[cidx]
            dqkv_ref[0, rows, :] = dq
            dqkv_ref[1, rows, :] = dk
            dqkv_ref[2, rows, :] = dv
            dg_ref[rows, :] += jnp.where(lane == LANE_G + h, dgc, 0.0) + jnp.where(lane == LANE_BETA + h, dbc, 0.0)

    qs, ks, vs, gs, a_s, gl_s = _prep_specs()
    return pl.pallas_call(
        body, name="gdn_prep_bwd", grid=(NCB, NGH), in_specs=[qs, ks, vs, gs, a_s, qs, qs, qs, qs, a_s, gl_s],
        out_specs=[pl.BlockSpec((3, CB * CHUNK, GHD), lambda i, h: (0, i, h)), gs],
        out_shape=[jax.ShapeDtypeStruct((3, S, DGDN), F32), jax.ShapeDtypeStruct((S, LANES), F32)],
        compiler_params=_cparams(),
    )(qkv, qkv, qkv, gates, t_inv, du, dw, dqd, dkd, da, dgl)


def _scan_specs():
    hs = pl.BlockSpec((S, GHD), lambda h: (0, h))
    a_s = pl.BlockSpec((None, S, CHUNK), lambda h: (h, 0, 0))
    gl_s = pl.BlockSpec((None, NCH, 1, LANES), lambda h: (h, 0, 0, 0))
    st_s = pl.BlockSpec((None, NCH, GHD, GHD), lambda h: (h, 0, 0, 0))
    gz_s = pl.BlockSpec((S, GHD), lambda h: (0, BLK_GZ + h))
    mix_s = pl.BlockSpec((S, GHD), lambda h: (0, NPAIR + h))
    return hs, a_s, gl_s, st_s, gz_s, mix_s


def _gdn_scan(u, w, qd, kd, a, gl, proj, w_norm, mix):
    def body(u_ref, w_ref, qd_ref, kd_ref, a_ref, gl_ref, z_ref, wn_ref, mix_in, mix_ref, o_ref, st_ref):
        del mix_in

        def step(ci, state):
            rows = pl.ds(pl.multiple_of(ci * CHUNK, CHUNK), CHUNK)
            st_ref[ci] = state
            vn = u_ref[rows, :] - _dot(w_ref[rows, :], state)
            o_ref[rows, :] = _dot(qd_ref[rows, :], state) + _dot(a_ref[rows, :], vn)
            return state * gl_ref[ci] + _dot(kd_ref[rows, :], vn, 0, 0)

        lax.fori_loop(0, NCH, step, jnp.zeros((GHD, GHD), F32))
        ov = o_ref[...]
        mix_ref[...] = (ov * _rms_scale(ov) * wn_ref[...] * _silu(z_ref[...])).astype(BF16)

    hs, a_s, gl_s, st_s, gz_s, mix_s = _scan_specs()
    return pl.pallas_call(
        body, name="gdn_scan", grid=(NGH,),
        in_specs=[hs, hs, hs, hs, a_s, gl_s, gz_s, pl.BlockSpec((1, GHD), lambda h: (0, 0)), pl.BlockSpec(memory_space=pl.ANY)],
        out_specs=[mix_s, hs, st_s],
        out_shape=[jax.ShapeDtypeStruct((S, D), BF16), jax.ShapeDtypeStruct((S, DGDN), F32),
                   jax.ShapeDtypeStruct((NGH, NCH, GHD, GHD), F32)],
        input_output_aliases={8: 0}, compiler_params=_cparams(),
    )(u, w, qd, kd, a, gl, proj, w_norm, mix)


def _gdn_scan_bwd(dmix, o, proj, w_norm, u, w, qd, kd, a, gl, states, dproj):
    def body(dy_ref, o_ref, z_ref, wn_ref, u_ref, w_ref, qd_ref, kd_ref, a_ref, gl_ref, st_ref, dproj_in,
             dz_ref, du_ref, dw_ref, dqd_ref, dkd_ref, da_ref, dgl_ref, dwn_ref, do_ref):
        del dproj_in
        h = pl.program_id(0)
        ov = o_ref[...]
        zv = z_ref[...]
        wn = wn_ref[...]
        g = dy_ref[...]
        sig = _sigmoid(zv)
        dz_ref[...] = (g * (ov * _rms_scale(ov) * wn) * sig * (1.0 + zv * (1.0 - sig))).astype(BF16)
        do, dwt = _rms_bwd(ov, wn, g * zv * sig)
        do_ref[...] = do

        @pl.when(h == 0)
        def _():
            dwn_ref[...] = jnp.zeros_like(dwn_ref)

        dwn_ref[...] += jnp.sum(dwt, axis=0, keepdims=True)

        def step(t, dstate):
            ci = NCH - 1 - t
            rows = pl.ds(pl.multiple_of(ci * CHUNK, CHUNK), CHUNK)
            state = st_ref[ci]
            dov = do_ref[rows, :]
            wv = w_ref[rows, :]
            kdv = kd_ref[rows, :]
            vn = u_ref[rows, :] - _dot(wv, state)
            dvn = _dot(a_ref[rows, :], dov, 0, 0) + _dot(kdv, dstate)
            da_ref[rows, :] = _dot(dov, vn, 1, 1)
            dqd_ref[rows, :] = _dot(dov, state, 1, 1)
            dkd_ref[rows, :] = _dot(vn, dstate, 1, 1)
            dgl = jnp.sum(jnp.sum(dstate * state, axis=1, keepdims=True), axis=0, keepdims=True)
            dgl_ref[ci] = jnp.broadcast_to(dgl, (1, LANES))
            du_ref[rows, :] = dvn
            dw_ref[rows, :] = -_dot(dvn, state, 1, 1)
            return dstate * gl_ref[ci] + _dot(qd_ref[rows, :], dov, 0, 0) - _dot(wv, dvn, 0, 0)

        lax.fori_loop(0, NCH, step, jnp.zeros((GHD, GHD), F32))

    hs, a_s, gl_s, st_s, gz_s, mix_s = _scan_specs()
    vec = pl.BlockSpec((1, GHD), lambda h: (0, 0))
    tok = jax.ShapeDtypeStruct((S, DGDN), F32)
    return pl.pallas_call(
        body, name="gdn_scan_bwd", grid=(NGH,),
        in_specs=[mix_s, hs, gz_s, vec, hs, hs, hs, hs, a_s, gl_s, st_s, pl.BlockSpec(memory_space=pl.ANY)],
        out_specs=[gz_s, hs, hs, hs, hs, a_s, gl_s, vec],
        out_shape=[jax.ShapeDtypeStruct((S, DPROJ_PAD), BF16), tok, tok, tok, tok,
                   jax.ShapeDtypeStruct((NGH, S, CHUNK), F32), jax.ShapeDtypeStruct((NGH, NCH, 1, LANES), F32),
                   jax.ShapeDtypeStruct((1, GHD), F32)],
        input_output_aliases={11: 0}, scratch_shapes=[pltpu.VMEM((S, GHD), F32)], compiler_params=_cparams(),
    )(dmix, o, proj, w_norm, u, w, qd, kd, a, gl, states, dproj)


def _place():
    return lax.axis_index("x"), lax.axis_index("y"), lax.axis_index("c")


def _other_chips(x, y):
    return [(1 - x, y), (x, 1 - y), (1 - x, 1 - y)]


HBM = pl.BlockSpec(memory_space=pltpu.HBM)
VMEM = pl.BlockSpec(memory_space=pltpu.VMEM)


def _half_rows(ref_or_rows, half):
    rows = ref_or_rows // 2
    return pl.ds(pl.multiple_of(half * rows, rows), rows)


class _Exchange:
    def __init__(self, inputs, out_shape, n_sems, start, finish):
        self.inputs, self.out_shape, self.n_sems, self.start, self.finish = inputs, out_shape, n_sems, start, finish

    def sem_shapes(self):
        return [pltpu.SemaphoreType.DMA((self.n_sems,)), pltpu.SemaphoreType.DMA((self.n_sems,))]

    def split(self, refs):
        n_in, n_out = len(self.inputs), len(self.out_shape)
        return refs[:n_in], refs[n_in:n_in + n_out], refs[n_in + n_out], refs[n_in + n_out + 1]


def _run_exchange(ex, name):
    def body(*refs):
        parts = ex.split(refs)
        ex.start(*parts)
        ex.finish(*parts)

    return pl.pallas_call(
        body, name=name, in_specs=[HBM] * len(ex.inputs), out_specs=[HBM] * len(ex.out_shape), out_shape=ex.out_shape,
        scratch_shapes=ex.sem_shapes(), compiler_params=_cparams(),
    )(*ex.inputs)


def _allgather_exchange(shards, whole=()):
    n, nw = len(shards), len(whole)

    def plan(src, outs, send_sems, recv_sems):
        x, y, c = _place()
        chips = _other_chips(x, y)
        chip_ids = [2 * ch[0] + ch[1] for ch in chips]

        def copy(a, k, chip_index, half, to, from_src):
            rows = _half_rows(src[a].shape[0], half)
            dst = outs[a].at[chip_index, rows]
            return pltpu.make_async_remote_copy(
                src_ref=src[a].at[rows] if from_src else dst, dst_ref=dst, send_sem=send_sems.at[6 * a + k],
                recv_sem=recv_sems.at[6 * a + k], device_id=to, device_id_type=MESH)

        def whole_copy(b, k, chip_index, to):
            return pltpu.make_async_remote_copy(
                src_ref=src[n + b], dst_ref=outs[n + b].at[chip_index], send_sem=send_sems.at[6 * n + 3 * b + k],
                recv_sem=recv_sems.at[6 * n + 3 * b + k], device_id=to, device_id_type=MESH)

        me, sibling = (x, y, c), (x, y, 1 - c)
        first = [copy(a, j, 2 * x + y, c, (*chips[j], c), True) for a in range(n) for j in range(3)]
        first += [whole_copy(b, j, 2 * x + y, (*chips[j], c)) for b in range(nw) for j in range(3)]
        landing = [copy(a, j, chip_ids[j], c, me, False) for a in range(n) for j in range(3)]
        passed = [copy(a, 3 + j, chip_ids[j], c, sibling, False) for a in range(n) for j in range(3)]
        arriving = [copy(a, 3 + j, chip_ids[j], 1 - c, me, False) for a in range(n) for j in range(3)]
        arriving += [whole_copy(b, j, chip_ids[j], me) for b in range(nw) for j in range(3)]
        return first, landing, passed, arriving

    def start(*refs):
        for cp in plan(*refs)[0]:
            cp.start()

    def finish(*refs):
        first, landing, passed, arriving = plan(*refs)
        for lands, onward in zip(landing, passed):
            lands.wait_recv()
            onward.start()
        for cp in arriving:
            cp.wait_recv()
        for cp in first + passed:
            cp.wait_send()

    out_shape = [jax.ShapeDtypeStruct((NCHIP,) + s.shape, s.dtype) for s in list(shards) + list(whole)]
    return _Exchange(list(shards) + list(whole), out_shape, 6 * n + 3 * nw, start, finish)


def _with_own(gathered, own):
    x, y, _ = _place()
    return lax.dynamic_update_index_in_dim(gathered, own, 2 * x + y, axis=0)


def _simple_exchange(inputs, out_shape, copies_of):
    def start(*refs):
        for cp in copies_of(*refs):
            cp.start()

    def finish(*refs):
        for cp in copies_of(*refs):
            cp.wait()

    return _Exchange(list(inputs), out_shape, len(out_shape) * 3, start, finish)


def _pair_exchange(grads):
    def copies_of(src, outs, send_sems, recv_sems):
        x, y, c = _place()
        return [pltpu.make_async_remote_copy(
            src_ref=src[a].at[:, _half_rows(src[a].shape[1], 1 - c)], dst_ref=outs[a], send_sem=send_sems.at[a],
            recv_sem=recv_sems.at[a], device_id=(x, y, 1 - c), device_id_type=MESH) for a in range(len(src))]

    return _simple_exchange(
        grads, [jax.ShapeDtypeStruct((g.shape[0], g.shape[1] // 2, g.shape[2]), g.dtype) for g in grads], copies_of)


def _pair_sum(grads, theirs, name):
    n = len(grads)

    def body(*refs):
        south = lax.axis_index("c") == 0
        for a in range(n):
            g = refs[a][...]
            half = g.shape[0] // 2
            mine = jnp.where(south, g[:half], g[half:])
            refs[2 * n + a][...] = (mine.astype(F32) + refs[n + a][...].astype(F32)).astype(BF16)

    def specs(arrs):
        return [pl.BlockSpec((None,) + g.shape[1:], lambda j: (j, 0, 0)) for g in arrs]

    return pl.pallas_call(
        body, name=name, grid=(NCHIP,), in_specs=specs(grads) + specs(theirs), out_specs=specs(theirs),
        out_shape=[jax.ShapeDtypeStruct(g.shape, BF16) for g in theirs], compiler_params=_cparams(),
    )(*grads, *theirs)


def _chip_exchange(parts):
    def copies_of(src, outs, send_sems, recv_sems):
        x, y, c = _place()
        return [pltpu.make_async_remote_copy(
            src_ref=src[a].at[2 * chip[0] + chip[1]], dst_ref=outs[a].at[k], send_sem=send_sems.at[3 * a + k],
            recv_sem=recv_sems.at[3 * a + k], device_id=(*chip, c), device_id_type=MESH)
            for a in range(len(src)) for k, chip in enumerate(_other_chips(x, y))]

    return _simple_exchange(parts, [jax.ShapeDtypeStruct((NCHIP - 1,) + p.shape[1:], p.dtype) for p in parts], copies_of)


def _chip_sum(parts, received):
    n = len(parts)
    steps = 4

    def body(*refs):
        chip = 2 * lax.axis_index("x") + lax.axis_index("y")
        for a in range(n):
            p, r = refs[a], refs[n + a]
            own = jnp.where(chip == 0, p[0], jnp.where(chip == 1, p[1], jnp.where(chip == 2, p[2], p[3])))
            refs[2 * n + a][...] = ((own.astype(F32) + r[0].astype(F32)) + r[1].astype(F32)) + r[2].astype(F32)

    def specs(arrs):
        return [pl.BlockSpec((g.shape[0], g.shape[1] // steps, g.shape[2]), lambda i: (0, i, 0)) for g in arrs]

    out_specs = [pl.BlockSpec((g.shape[1] // steps, g.shape[2]), lambda i: (i, 0)) for g in parts]
    return pl.pallas_call(
        body, name="grads_chip_sum", grid=(steps,), in_specs=specs(parts) + specs(received), out_specs=out_specs,
        out_shape=[jax.ShapeDtypeStruct(g.shape[1:], F32) for g in parts], compiler_params=_cparams(),
    )(*parts, *received)


def _pair_share(halves):
    def copies_of(src, outs, send_sems, recv_sems):
        x, y, c = _place()
        return [pltpu.make_async_remote_copy(
            src_ref=src[a], dst_ref=outs[a], send_sem=send_sems.at[a], recv_sem=recv_sems.at[a],
            device_id=(x, y, 1 - c), device_id_type=MESH) for a in range(len(src))]

    return _simple_exchange(halves, [jax.ShapeDtypeStruct(h.shape, F32) for h in halves], copies_of)


def _adamw_math(w, g, m, v):
    nm = ADAM_B1 * m + (1.0 - ADAM_B1) * g
    nv = ADAM_B2 * v + (1.0 - ADAM_B2) * jnp.square(g)
    m_hat = nm / (1.0 - ADAM_B1 ** ADAM_STEP)
    v_hat = nv / (1.0 - ADAM_B2 ** ADAM_STEP)
    return -ADAM_LR * (m_hat / (jnp.sqrt(v_hat) + ADAM_EPS) + ADAM_WD * w), nm, nv


def _adamw_big(ws, g_mine, g_theirs, ms, vs):
    n = len(ws)
    steps = 8

    def body(*refs):
        own_half = (pl.program_id(0) // (steps // 2)) == lax.axis_index("c")
        for a in range(n):
            w = refs[a][...]
            g = jnp.where(own_half, refs[n + a][...], refs[2 * n + a][...])[:, :w.shape[1]]
            d, nm, nv = _adamw_math(w, g, refs[3 * n + a][...], refs[4 * n + a][...])
            refs[5 * n + a][...] = g
            refs[6 * n + a][...] = d
            refs[7 * n + a][...] = nm
            refs[8 * n + a][...] = nv

    specs = [pl.BlockSpec((w.shape[0] // steps, w.shape[1]), lambda i: (i, 0)) for w in ws]
    half_specs = [pl.BlockSpec((g.shape[0] // (steps // 2), g.shape[1]), lambda i: (i % (steps // 2), 0)) for g in g_mine]
    shapes = [jax.ShapeDtypeStruct(w.shape, F32) for w in ws]
    res = pl.pallas_call(
        body, name="adamw_big", grid=(steps,), in_specs=specs + half_specs * 2 + specs * 2, out_specs=specs * 4,
        out_shape=shapes * 4, compiler_params=_cparams(),
    )(*ws, *g_mine, *g_theirs, *ms, *vs)
    return res[:n], res[n:2 * n], res[2 * n:3 * n], res[3 * n:]


def _adamw_in(w, g_mine, g_theirs, m, v):
    half = D // 2

    def body(w_ref, gm_ref, gt_ref, m_ref, v_ref, g_out, d_out, nm_out, nv_out, g_ref):
        south = lax.axis_index("c") == 0
        g_ref[0:half, :] = jnp.where(south, gm_ref[...], gt_ref[...])
        g_ref[half:D, :] = jnp.where(south, gt_ref[...], gm_ref[...])
        g = g_ref[0:CW, :]
        d, nm, nv = _adamw_math(w_ref[...], g, m_ref[...], v_ref[...])
        g_out[...] = g
        d_out[...] = d
        nm_out[...] = nm
        nv_out[...] = nv

    spec = pl.BlockSpec((CW, LANES), lambda i: (0, i))
    half_spec = pl.BlockSpec((half, LANES), lambda i: (0, i))
    return pl.pallas_call(
        body, name="adamw_in", grid=(D // LANES,), in_specs=[spec, half_spec, half_spec, spec, spec], out_specs=[spec] * 4,
        out_shape=[jax.ShapeDtypeStruct((CW, D), F32)] * 4, scratch_shapes=[pltpu.VMEM((D, LANES), F32)],
        compiler_params=_cparams(),
    )(w, g_mine, g_theirs, m, v)


NORM_NAMES = ("pre_mix_norm", "post_mix_norm", "pre_mlp_norm", "post_mlp_norm")
SMALL_NAMES = NORM_NAMES + ("gdn_conv_w", "fox_f_bias", "gdn_dt_bias", "gdn_a_log", "fox_out_norm", "gdn_out_norm")
CONV_COLS = 3 * DGDN // NCHIP


def _small_allreduce(d_norms, d_conv, sums, d_fox_norm, d_gdn_norm):
    def body(*refs):
        dn_ref, dconv_ref, sums_ref, dfn_ref, dgn_ref = refs[:5]
        outs = refs[5:10]
        g_norms, g_conv, g_sums, g_fn, g_gn, send_sems, recv_sems, local_sem = refs[10:]
        x, y, c = _place()
        me = 4 * x + 2 * y + c
        g_norms[me] = dn_ref[...]
        g_sums[me] = sums_ref[...]
        g_fn[me] = dfn_ref[...]
        g_gn[me] = dgn_ref[...]

        def conv_cols(chip_index):
            return dconv_ref.at[:, pl.ds(pl.multiple_of(chip_index * CONV_COLS, LANES), CONV_COLS)]

        own = pltpu.make_async_copy(conv_cols(2 * x + y), g_conv.at[me], local_sem)
        own.start()
        copies = []
        for k in range(1, NDEV):
            px, py, pc = x ^ ((k >> 2) & 1), y ^ ((k >> 1) & 1), c ^ (k & 1)
            pairs = [(dn_ref, g_norms), (conv_cols(2 * px + py), g_conv), (sums_ref, g_sums), (dfn_ref, g_fn), (dgn_ref, g_gn)]
            for a, (src, dst) in enumerate(pairs):
                cp = pltpu.make_async_remote_copy(
                    src_ref=src, dst_ref=dst.at[me], send_sem=send_sems.at[5 * (k - 1) + a],
                    recv_sem=recv_sems.at[5 * (k - 1) + a], device_id=(px, py, pc), device_id_type=MESH)
                cp.start()
                copies.append(cp)
        own.wait()
        for cp in copies:
            cp.wait()

        def total(buf):
            acc = buf[0]
            for i in range(1, NDEV):
                acc = acc + buf[i]
            return acc

        for out, buf in zip(outs, (g_norms, g_conv, g_sums, g_fn, g_gn)):
            out[...] = total(buf)

    n_sem = 5 * (NDEV - 1)
    shapes = [(4, D), (CONV_K, CONV_COLS), (8, LANES), (1, LANES), (1, LANES)]
    return pl.pallas_call(
        body, name="small_allreduce", in_specs=[VMEM] * 5, out_specs=[VMEM] * 5,
        out_shape=[jax.ShapeDtypeStruct(s, F32) for s in shapes],
        scratch_shapes=[pltpu.VMEM((NDEV,) + s, F32) for s in shapes]
        + [pltpu.SemaphoreType.DMA((n_sem,)), pltpu.SemaphoreType.DMA((n_sem,)), pltpu.SemaphoreType.DMA],
        compiler_params=_cparams(),
    )(d_norms, d_conv, sums, d_fox_norm, d_gdn_norm)


def _small_adamw(totals, ws, ms, vs):
    n = len(SMALL_NAMES)

    def body(*refs):
        t_norms, t_conv, t_sums, t_fn, t_gn = [r[...] for r in refs[:5]]
        w_refs, m_refs, v_refs = refs[5:5 + n], refs[5 + n:5 + 2 * n], refs[5 + 2 * n:5 + 3 * n]
        outs = refs[5 + 3 * n:]
        grads = [t_norms[i:i + 1, :] for i in range(4)] + [
            t_conv, t_sums[0:1, 0:NFH], t_sums[1:2, 0:NGH], t_sums[2:3, 0:NGH], t_fn[:, 0:FHD], t_gn]
        for a in range(n):
            d, nm, nv = _adamw_math(w_refs[a][...], grads[a], m_refs[a][...], v_refs[a][...])
            outs[a][...] = grads[a]
            outs[n + a][...] = d
            outs[2 * n + a][...] = nm
            outs[3 * n + a][...] = nv

    def whole(arr):
        return pl.BlockSpec(arr.shape, lambda i: (0, 0))

    res = pl.pallas_call(
        body, name="small_adamw", grid=(1,), in_specs=[whole(t) for t in totals] + [whole(w) for w in ws] * 3,
        out_specs=[whole(w) for w in ws] * 4, out_shape=[jax.ShapeDtypeStruct(w.shape, F32) for w in ws] * 4,
        compiler_params=_cparams(),
    )(*totals, *ws, *ms, *vs)
    return res[:n], res[n:2 * n], res[2 * n:3 * n], res[3 * n:]


CW = DPROJ // NCHIP
PROJ_RUNS = tuple((part * DFOX + hp * LANES, part * DFOX + (hp + 1) * LANES, (3 * hp + part) * LANES)
                  for hp in range(NPAIR) for part in range(3)) + (
    (1536, 1544, BLK_SMALL * LANES), (1544, 3080, BLK_GDN * LANES), (3080, 3088, BLK_SMALL * LANES + 8),
    (3088, 3600, BLK_GZ * LANES))


def _proj_pieces():
    pieces = []
    for lo, hi, at in PROJ_RUNS:
        while lo < hi:
            j = lo // CW
            end = min(hi, (j + 1) * CW)
            pieces.append((j, lo - j * CW, at, end - lo))
            at, lo = at + end - lo, end
    return pieces


RT = 256


def _to_padded_rows(gathered):
    def body(src_ref, out_ref, blocks_ref, rows_ref):
        blocks_ref[...] = src_ref[...].astype(F32)
        rows_ref[...] = jnp.zeros_like(rows_ref)
        for j, start, at, n in _proj_pieces():
            rows_ref[at:at + n, :] = blocks_ref[j, start:start + n, :]
        out_ref[...] = rows_ref[...].astype(out_ref.dtype)

    return pl.pallas_call(
        body, name="proj_rows_in", grid=(D // RT,), in_specs=[pl.BlockSpec((NCHIP, D, RT), lambda i: (0, 0, i))],
        out_specs=pl.BlockSpec((DPROJ_PAD, RT), lambda i: (0, i)), out_shape=jax.ShapeDtypeStruct((DPROJ_PAD, D), gathered.dtype),
        scratch_shapes=[pltpu.VMEM((NCHIP, D, RT), F32), pltpu.VMEM((DPROJ_PAD, RT), F32)], compiler_params=_cparams(),
    )(gathered)


def _from_padded_rows(w):
    def body(src_ref, out_ref, rows_ref, blocks_ref):
        rows_ref[...] = src_ref[...].astype(F32)
        blocks_ref[...] = jnp.zeros_like(blocks_ref)
        for j, start, at, n in _proj_pieces():
            blocks_ref[j, start:start + n, :] = rows_ref[at:at + n, :]
        out_ref[...] = blocks_ref[...].astype(out_ref.dtype)

    return pl.pallas_call(
        body, name="proj_rows_out", grid=(D // RT,), in_specs=[pl.BlockSpec((DPROJ_PAD, RT), lambda i: (0, i))],
        out_specs=pl.BlockSpec((NCHIP, D, RT), lambda i: (0, 0, i)), out_shape=jax.ShapeDtypeStruct((NCHIP, D, D), w.dtype),
        scratch_shapes=[pltpu.VMEM((DPROJ_PAD, RT), F32), pltpu.VMEM((NCHIP, D, RT), F32)], compiler_params=_cparams(),
    )(w)


def _local_step(x, target, win_p, late_weights, reduce_late, reduce_in, pre_mix_norm, fox_f_bias, fox_out_norm, conv_w,
                gdn_a_log, gdn_dt_bias, gdn_out_norm, post_mix_norm, pre_mlp_norm, post_mlp_norm):
    bias_vec = jnp.zeros((1, LANES), F32).at[0, 0:NFH].set(fox_f_bias).at[0, LANE_G:LANE_G + NGH].set(gdn_dt_bias)
    alog_vec = jnp.zeros((1, LANES), F32).at[0, LANE_G:LANE_G + NGH].set(gdn_a_log)
    w2 = jnp.concatenate([fox_out_norm, fox_out_norm], axis=1)

    h = _pre_norm(x, pre_mix_norm)
    proj = _matmul(h, win_p, tb=True, tm=2048, tn=768, tk=1024, name="mm_proj")
    gates = _gates(proj, bias_vec, alog_vec)
    mix, fox_o, lse, late_a = _fox_fwd(proj, gates, w2, exchange=late_weights[0])
    qkv = _gdn_pre(proj, conv_w)
    (u, w, qd, kd, a_intra, gl, t_inv), late_b = _gdn_prep(qkv, gates, exchange=late_weights[1])
    wout, wup3, wdown = late_weights[2](late_a, late_b)
    mix, gdn_raw, states = _gdn_scan(u, w, qd, kd, a_intra, gl, proj, gdn_out_norm, mix)
    mixed = _matmul(mix, wout, tm=2048, tk=1024, name="mm_out")
    x1, h2 = _post_mix(x, mixed, post_mix_norm, pre_mlp_norm)

    def relu2(acc):
        r = jnp.maximum(acc, 0.0)
        return acc, r * r

    up, act = _matmul(h2, wup3, b3=True, tm=1024, tn=1024, tk=1024, out_dtypes=(F32, BF16), epilogue=relu2, name="mm_up")
    y = _matmul(act, wdown, tm=2048, tk=1024, name="mm_down")
    dx2, dy, d_post_mlp, loss_row = _loss_head(x1, y, post_mlp_norm, target)

    dwdown = _matmul(act, dy, ta=True, tm=1024, tn=1024, tk=2048, out_dtypes=(BF16,), name="mm_dwdown")

    def relu2_bwd(acc, upv):
        return (acc * 2.0 * jnp.maximum(upv, 0.0),)

    dup = _matmul(dy, wdown, tb=True, tm=1024, tn=1024, tk=1024, out_dtypes=(BF16,), extra=(up,), epilogue=relu2_bwd,
                  name="mm_dact")
    dwup3 = _matmul(h2, dup, ta=True, tm=1024, tn=1024, tk=2048, out_dtypes=(BF16,), o3=True, name="mm_dwup")
    dh2 = _matmul(dup, wup3, tb=True, b3=True, tm=2048, tk=1024, name="mm_dh2")
    dx1, dmixed, d_pre_mlp, d_post_mix = _mid_bwd(dh2, x1, pre_mlp_norm, dx2, mixed, post_mix_norm)
    dwout = _matmul(mix, dmixed, ta=True, tm=1024, tn=1024, tk=2048, out_dtypes=(BF16,), name="mm_dwout")
    dmix = _matmul(dmixed, wout, tb=True, tm=2048, tk=1024, name="mm_dmix")

    dfox, delta, d_fox_norm = _fox_norm_bwd(fox_o, dmix, w2)
    dproj, dcum_fox, reduced_late = _fox_bwd(proj, dfox, gates, lse, delta, exchange=reduce_late(dwout, dwup3, dwdown))
    dproj, du, dw, dqd, dkd, da, dgl, d_gdn_norm = _gdn_scan_bwd(dmix, gdn_raw, proj, gdn_out_norm, u, w, qd, kd,
                                                                 a_intra, gl, states, dproj)
    dqkv, dgates_gdn = _gdn_prep_bwd(qkv, gates, t_inv, du, dw, dqd, dkd, da, dgl)
    dproj, d_conv = _gdn_pre_bwd(proj, conv_w, dqkv, dproj)
    dproj, sums = _gates_bwd(proj, bias_vec, alog_vec, dgates_gdn, dcum_fox, dproj)

    dwin_p = _matmul(dproj, h, ta=True, tm=1280, tn=1024, tk=2048, out_dtypes=(BF16,), name="mm_dwin")
    exchange_in = reduce_in(dwin_p)
    dh = _matmul(dproj, win_p, tm=2048, tk=1280, name="mm_dh", exchange=exchange_in)
    dh, reduced_in = dh if exchange_in is not None else (dh, [])
    grad_x, d_pre_mix = _pre_norm_bwd(dh, x, pre_mix_norm, dx1)

    d_norms = jnp.concatenate([d_pre_mix, d_post_mix, d_pre_mlp, d_post_mlp], axis=0)
    return loss_row[0, 0], grad_x, (d_norms, d_conv, sums, d_fox_norm, d_gdn_norm), reduced_late, reduced_in


def kernel(x, pre_mix_norm, w_in, fox_f_bias, fox_out_norm, gdn_conv_w, gdn_a_log, gdn_dt_bias, gdn_out_norm, w_out, post_mix_norm, pre_mlp_norm, w_up, w_down, post_mlp_norm, loss_target, m_pre_mix_norm, m_w_in, m_fox_f_bias, m_fox_out_norm, m_gdn_conv_w, m_gdn_a_log, m_gdn_dt_bias, m_gdn_out_norm, m_w_out, m_post_mix_norm, m_pre_mlp_norm, m_w_up, m_w_down, m_post_mlp_norm, v_pre_mix_norm, v_w_in, v_fox_f_bias, v_fox_out_norm, v_gdn_conv_w, v_gdn_a_log, v_gdn_dt_bias, v_gdn_out_norm, v_w_out, v_post_mix_norm, v_pre_mlp_norm, v_w_up, v_w_down, v_post_mlp_norm):
    weights = dict(pre_mix_norm=pre_mix_norm, w_in=w_in, fox_f_bias=fox_f_bias, fox_out_norm=fox_out_norm, gdn_conv_w=gdn_conv_w,
                   gdn_a_log=gdn_a_log, gdn_dt_bias=gdn_dt_bias, gdn_out_norm=gdn_out_norm, w_out=w_out, post_mix_norm=post_mix_norm,
                   pre_mlp_norm=pre_mlp_norm, w_up=w_up, w_down=w_down, post_mlp_norm=post_mlp_norm)
    m_in = dict(pre_mix_norm=m_pre_mix_norm, w_in=m_w_in, fox_f_bias=m_fox_f_bias, fox_out_norm=m_fox_out_norm, gdn_conv_w=m_gdn_conv_w,
                gdn_a_log=m_gdn_a_log, gdn_dt_bias=m_gdn_dt_bias, gdn_out_norm=m_gdn_out_norm, w_out=m_w_out, post_mix_norm=m_post_mix_norm,
                pre_mlp_norm=m_pre_mlp_norm, w_up=m_w_up, w_down=m_w_down, post_mlp_norm=m_post_mlp_norm)
    v_in = dict(pre_mix_norm=v_pre_mix_norm, w_in=v_w_in, fox_f_bias=v_fox_f_bias, fox_out_norm=v_fox_out_norm, gdn_conv_w=v_gdn_conv_w,
                gdn_a_log=v_gdn_a_log, gdn_dt_bias=v_gdn_dt_bias, gdn_out_norm=v_gdn_out_norm, w_out=v_w_out, post_mix_norm=v_post_mix_norm,
                pre_mlp_norm=v_pre_mlp_norm, w_up=v_w_up, w_down=v_w_down, post_mlp_norm=v_post_mlp_norm)
    order_w = ("pre_mix_norm", "w_in", "fox_f_bias", "fox_out_norm", "gdn_conv_w", "gdn_a_log", "gdn_dt_bias", "gdn_out_norm", "w_out",
               "post_mix_norm", "pre_mlp_norm", "w_up", "w_down", "post_mlp_norm")
    big = ("w_in", "w_out", "w_up", "w_down")

    def row(v):
        return v if v.ndim == 2 else v.reshape(1, -1)

    win_shard = jnp.pad(w_in.T.astype(BF16), ((0, D - CW), (0, 0)))
    win_g, conv_g = _run_exchange(_allgather_exchange([win_shard], whole=[gdn_conv_w]), "weights_allgather_in")
    win_p = _to_padded_rows(_with_own(win_g, win_shard))
    conv_full = _with_own(conv_g, gdn_conv_w).transpose(1, 0, 2).reshape(CONV_K, 3 * DGDN)
    late_shards = [weights[n].astype(BF16) for n in big[1:]]

    def resolve_late(gathered_a, gathered_b):
        wout_g, wup3, wdown_g = [_with_own(g, own) for g, own in zip(list(gathered_a) + list(gathered_b), late_shards)]
        return wout_g.reshape(D, D), wup3, wdown_g.reshape(DFF, D)

    pair_sums = {}

    def pair_reduced(names, blocks):
        theirs = _run_exchange(_pair_exchange(blocks), "grads_pair_exchange_" + names[0])
        for n, s in zip(names, _pair_sum(blocks, theirs, "grads_pair_sum_" + names[0])):
            pair_sums[n] = s
        return _chip_exchange([pair_sums[n] for n in names])

    def reduce_late(dwout, dwup3, dwdown):
        return pair_reduced(big[1:], [dwout.reshape(NCHIP, D // NCHIP, D), dwup3, dwdown.reshape(NCHIP, DFF // NCHIP, D)])

    def reduce_in(dwin_p):
        return pair_reduced(big[:1], [_from_padded_rows(dwin_p)])

    loss_local, grad_x, small, received_late, received_in = _local_step(
        x[0], loss_target[0], win_p, (_allgather_exchange(late_shards[:2]), _allgather_exchange(late_shards[2:]), resolve_late),
        reduce_late, reduce_in, row(pre_mix_norm), fox_f_bias, row(fox_out_norm), conv_full, gdn_a_log, gdn_dt_bias,
        row(gdn_out_norm), row(post_mix_norm), row(pre_mlp_norm), row(post_mlp_norm))
    loss = lax.psum(loss_local, ("x", "y", "c"))

    g_mine = _chip_sum([pair_sums[n] for n in big], list(received_in) + list(received_late))
    g_theirs = _run_exchange(_pair_share(g_mine), "grads_pair_share")

    g_small, d_small, nm_small, nv_small = _small_adamw(
        _small_allreduce(*small), [row(weights[n]) for n in SMALL_NAMES], [row(m_in[n]) for n in SMALL_NAMES],
        [row(v_in[n]) for n in SMALL_NAMES])

    g_big, d_big, nm_big, nv_big = _adamw_big([weights[n] for n in big[1:]], g_mine[1:], g_theirs[1:],
                                              [m_in[n] for n in big[1:]], [v_in[n] for n in big[1:]])
    in_t = _adamw_in(w_in.T, g_mine[0], g_theirs[0], m_w_in.T, v_w_in.T)

    grads, delta, new_m, new_v = {}, {}, {}, {}
    grads["w_in"], delta["w_in"], new_m["w_in"], new_v["w_in"] = [t.T for t in in_t]
    for i, n in enumerate(big[1:]):
        grads[n], delta[n], new_m[n], new_v[n] = g_big[i], d_big[i], nm_big[i], nv_big[i]
    for i, n in enumerate(SMALL_NAMES):
        shape = weights[n].shape
        grads[n], delta[n], new_m[n], new_v[n] = (g_small[i].reshape(shape), d_small[i].reshape(shape),
                                                  nm_small[i].reshape(shape), nv_small[i].reshape(shape))
    return (loss, grad_x[None], *[grads[n] for n in order_w], *[delta[n] for n in order_w], *[new_m[n] for n in order_w],
            *[new_v[n] for n in order_w])
```

```python
import jax
import jax.numpy as jnp
from jax import lax
from jax.experimental import pallas as pl
from jax.experimental.pallas import tpu as pltpu

F32 = jnp.float32
BF16 = jnp.bfloat16
MESH = pl.DeviceIdType.MESH

S = 2048
D = 1024
NFH, FHD = 8, 64
NPAIR = NFH // 2
NGH, GHD = 4, 128
DFOX = NFH * FHD
DGDN = NGH * GHD
CHUNK = 64
NCH = S // CHUNK
CONV_K = 4
DFF = 4 * D
EPS = 1e-6
DPROJ = 3600
LANES = 128
DPROJ_PAD = 3840
BLK_GDN = 12
BLK_GZ = 24
BLK_SMALL = 28
NCHIP = 4
NDEV = 8
VMEM_LIMIT = 56 * 1024 * 1024

ADAM_LR = 0.001
ADAM_B1 = 0.9
ADAM_B2 = 0.999
ADAM_EPS = 1e-08
ADAM_WD = 0.01
ADAM_STEP = 10


def _cparams(**kw):
    return pltpu.CompilerParams(vmem_limit_bytes=VMEM_LIMIT, **kw)


def _dn(ca, cb):
    return (((ca,), (cb,)), ((), ()))


def _dot(a, b, ca=1, cb=0):
    return lax.dot_general(a.astype(BF16), b.astype(BF16), _dn(ca, cb), preferred_element_type=F32)


def _hdot(a, b, ca=1, cb=0):
    return lax.dot_general(a.astype(F32), b.astype(F32), _dn(ca, cb), precision=lax.Precision.HIGHEST,
                           preferred_element_type=F32)


def _dot3(a, b, ca=1, cb=0):
    a_hi, b_hi = a.astype(BF16), b.astype(BF16)
    a_lo, b_lo = (a - a_hi.astype(F32)).astype(BF16), (b - b_hi.astype(F32)).astype(BF16)
    dn = _dn(ca, cb)
    return (lax.dot_general(a_hi, b_hi, dn, preferred_element_type=F32)
            + (lax.dot_general(a_hi, b_lo, dn, preferred_element_type=F32)
               + lax.dot_general(a_lo, b_hi, dn, preferred_element_type=F32)))


@jax.custom_vjp
def _mm_nn(a, b):
    return _dot(a, b, 1, 0)


def _mm_nn_fwd(a, b):
    return _dot(a, b, 1, 0), (a, b)


def _mm_nn_bwd(res, g):
    a, b = res
    return _dot(g, b, 1, 1), _dot(a, g, 0, 0)


_mm_nn.defvjp(_mm_nn_fwd, _mm_nn_bwd)


@jax.custom_vjp
def _mm_nt(a, b):
    return _dot(a, b, 1, 1)


def _mm_nt_fwd(a, b):
    return _dot(a, b, 1, 1), (a, b)


def _mm_nt_bwd(res, g):
    a, b = res
    return _dot(g, b, 1, 0), _dot(g, a, 0, 0)


_mm_nt.defvjp(_mm_nt_fwd, _mm_nt_bwd)


@jax.custom_vjp
def _saved_inverse(m, t_inv):
    del m
    return t_inv


def _saved_inverse_fwd(m, t_inv):
    del m
    return t_inv, t_inv


def _saved_inverse_bwd(t_inv, g):
    return -_dot3(_dot3(t_inv, g, 0, 0), t_inv, 1, 1), jnp.zeros_like(t_inv)


_saved_inverse.defvjp(_saved_inverse_fwd, _saved_inverse_bwd)


def _sigmoid(z):
    return 1.0 / (1.0 + jnp.exp(-z))


def _softplus(z):
    return jnp.maximum(z, 0.0) + jnp.log(1.0 + jnp.exp(-jnp.abs(z)))


def _silu(z):
    return z * _sigmoid(z)


def _rms_scale(x):
    return lax.rsqrt(jnp.mean(x * x, axis=-1, keepdims=True) + EPS)


def _rms_bwd(x, w, g):
    r = _rms_scale(x)
    gw = g * w
    dx = r * gw - x * (r * r * r) * jnp.mean(gw * x, axis=-1, keepdims=True)
    return dx, g * x * r


def _matmul(a, b, *, name, ta=False, tb=False, tm=512, tn=512, tk=512, out_dtypes=(F32,), b3=False, o3=False,
            extra=(), epilogue=None, exchange=None):
    m, k = (a.shape[1], a.shape[0]) if ta else a.shape
    if b3:
        n = b.shape[1] if tb else b.shape[0] * b.shape[2]
        kb = b.shape[0] * b.shape[2] if tb else b.shape[1]
    else:
        n, kb = (b.shape[0], b.shape[1]) if tb else (b.shape[1], b.shape[0])
    assert kb == k, (name, kb, k)
    tm, tn, tk = min(tm, m), min(tn, n), min(tk, k)
    assert m % tm == 0 and n % tn == 0 and k % tk == 0, (name, m, n, k, tm, tn, tk)
    nk = k // tk
    n_extra = len(extra)
    n_out = len(out_dtypes)
    grid = (m // tm, n // tn, nk)
    ex_in, ex_in_specs, ex_out_specs, ex_out_shape, ex_scratch = _hosted(exchange)

    def body(*refs):
        a_ref, b_ref = refs[0], refs[1]
        extra_refs = refs[2:2 + n_extra]
        first_out = 2 + n_extra + len(ex_in)
        out_refs = refs[first_out:first_out + n_out]
        ex_refs = refs[2 + n_extra:first_out] + refs[first_out + n_out:first_out + n_out + len(ex_out_shape)] + refs[-2:]
        step = [pl.program_id(d) for d in range(3)]

        if exchange is not None:
            @pl.when((step[0] == 0) & (step[1] == 0) & (step[2] == 0))
            def _():
                exchange.start(*exchange.split(ex_refs))

        def finish(acc):
            outs = (acc,) if epilogue is None else epilogue(acc, *[r[...] for r in extra_refs])
            for o_ref, val in zip(out_refs, outs):
                o_ref[...] = val.astype(o_ref.dtype)

        part = _dot(a_ref[...], b_ref[...], 0 if ta else 1, 1 if tb else 0)
        if nk == 1:
            finish(part)
        else:
            acc_ref = refs[first_out + n_out + len(ex_out_shape)]

            @pl.when(step[2] == 0)
            def _():
                acc_ref[...] = part

            @pl.when(step[2] > 0)
            def _():
                acc_ref[...] += part

            @pl.when(step[2] == nk - 1)
            def _():
                finish(acc_ref[...])

        if exchange is not None:
            @pl.when((step[0] == grid[0] - 1) & (step[1] == grid[1] - 1) & (step[2] == nk - 1))
            def _():
                exchange.finish(*exchange.split(ex_refs))

    a_spec = pl.BlockSpec((tk, tm), lambda i, j, kk: (kk, i)) if ta else pl.BlockSpec((tm, tk), lambda i, j, kk: (i, kk))
    if b3 and tb:
        assert b.shape[2] == tk
        b_spec = pl.BlockSpec((None, tn, tk), lambda i, j, kk: (kk, j, 0))
    elif b3:
        assert b.shape[2] == tn
        b_spec = pl.BlockSpec((None, tk, tn), lambda i, j, kk: (j, kk, 0))
    elif tb:
        b_spec = pl.BlockSpec((tn, tk), lambda i, j, kk: (j, kk))
    else:
        b_spec = pl.BlockSpec((tk, tn), lambda i, j, kk: (kk, j))
    tile = pl.BlockSpec((tm, tn), lambda i, j, kk: (i, j))
    out_specs = [tile] * n_out
    out_shape = [jax.ShapeDtypeStruct((m, n), dt) for dt in out_dtypes]
    if o3:
        out_specs[0] = pl.BlockSpec((None, tm, tn), lambda i, j, kk: (j, i, 0))
        out_shape[0] = jax.ShapeDtypeStruct((n // tn, m, tn), out_dtypes[0])
    res = pl.pallas_call(
        body, name=name, grid=grid,
        in_specs=[a_spec, b_spec] + [tile] * n_extra + ex_in_specs, out_specs=out_specs + ex_out_specs,
        out_shape=out_shape + ex_out_shape,
        scratch_shapes=([pltpu.VMEM((tm, tn), F32)] if nk > 1 else []) + ex_scratch,
        compiler_params=_cparams(),
    )(a, b, *extra, *ex_in)
    if exchange is not None:
        return (res[0] if n_out == 1 else res[:n_out]), res[n_out:]
    return res[0] if n_out == 1 else res


TR = 256


def _row_spec(cols):
    return pl.BlockSpec((TR, cols), lambda i: (i, 0))


def _vec_spec(cols):
    return pl.BlockSpec((1, cols), lambda i: (0, 0))


def _pre_norm(x, w):
    def body(x_ref, w_ref, h_ref):
        xv = x_ref[...]
        h_ref[...] = (xv * _rms_scale(xv) * w_ref[...]).astype(BF16)

    return pl.pallas_call(
        body, name="pre_norm", grid=(S // TR,), in_specs=[_row_spec(D), _vec_spec(D)], out_specs=_row_spec(D),
        out_shape=jax.ShapeDtypeStruct((S, D), BF16), compiler_params=_cparams(),
    )(x, w)


def _post_mix(x, mixed, w_post, w_pre_mlp):
    def body(x_ref, m_ref, wp_ref, wm_ref, x1_ref, h2_ref):
        mv = m_ref[...]
        x1 = x_ref[...] + mv * _rms_scale(mv) * wp_ref[...]
        x1_ref[...] = x1
        h2_ref[...] = (x1 * _rms_scale(x1) * wm_ref[...]).astype(BF16)

    return pl.pallas_call(
        body, name="post_mix", grid=(S // TR,),
        in_specs=[_row_spec(D), _row_spec(D), _vec_spec(D), _vec_spec(D)], out_specs=[_row_spec(D), _row_spec(D)],
        out_shape=[jax.ShapeDtypeStruct((S, D), F32), jax.ShapeDtypeStruct((S, D), BF16)], compiler_params=_cparams(),
    )(x, mixed, w_post, w_pre_mlp)


def _loss_head(x1, y, w_post_mlp, target):
    def body(x1_ref, y_ref, w_ref, t_ref, dx2_ref, dy_ref, dw_ref, loss_ref):
        i = pl.program_id(0)
        yv = y_ref[...]
        w = w_ref[...]
        x2 = x1_ref[...] + yv * _rms_scale(yv) * w
        err = x2 - t_ref[...]
        dx2 = err * (1.0 / D)
        dx2_ref[...] = dx2
        dy, dwt = _rms_bwd(yv, w, dx2)
        dy_ref[...] = dy.astype(BF16)

        @pl.when(i == 0)
        def _():
            dw_ref[...] = jnp.zeros_like(dw_ref)
            loss_ref[...] = jnp.zeros_like(loss_ref)

        dw_ref[...] += jnp.sum(dwt, axis=0, keepdims=True)
        part = 0.5 * jnp.sum(jnp.mean(err * err, axis=-1, keepdims=True), axis=0, keepdims=True)
        loss_ref[...] += jnp.broadcast_to(part, loss_ref.shape)

    return pl.pallas_call(
        body, name="loss_head", grid=(S // TR,),
        in_specs=[_row_spec(D), _row_spec(D), _vec_spec(D), _row_spec(D)],
        out_specs=[_row_spec(D), _row_spec(D), _vec_spec(D), _vec_spec(LANES)],
        out_shape=[jax.ShapeDtypeStruct((S, D), F32), jax.ShapeDtypeStruct((S, D), BF16),
                   jax.ShapeDtypeStruct((1, D), F32), jax.ShapeDtypeStruct((1, LANES), F32)],
        compiler_params=_cparams(),
    )(x1, y, w_post_mlp, target)


def _mid_bwd(dh2, x1, w_pre_mlp, dx2, mixed, w_post):
    def body(dh2_ref, x1_ref, wm_ref, dx2_ref, m_ref, wp_ref, dx1_ref, dm_ref, dwm_ref, dwp_ref):
        i = pl.program_id(0)
        dxa, dwm = _rms_bwd(x1_ref[...], wm_ref[...], dh2_ref[...])
        dx1 = dx2_ref[...] + dxa
        dx1_ref[...] = dx1
        dm, dwp = _rms_bwd(m_ref[...], wp_ref[...], dx1)
        dm_ref[...] = dm.astype(BF16)

        @pl.when(i == 0)
        def _():
            dwm_ref[...] = jnp.zeros_like(dwm_ref)
            dwp_ref[...] = jnp.zeros_like(dwp_ref)

        dwm_ref[...] += jnp.sum(dwm, axis=0, keepdims=True)
        dwp_ref[...] += jnp.sum(dwp, axis=0, keepdims=True)

    return pl.pallas_call(
        body, name="mid_bwd", grid=(S // TR,),
        in_specs=[_row_spec(D), _row_spec(D), _vec_spec(D), _row_spec(D), _row_spec(D), _vec_spec(D)],
        out_specs=[_row_spec(D), _row_spec(D), _vec_spec(D), _vec_spec(D)],
        out_shape=[jax.ShapeDtypeStruct((S, D), F32), jax.ShapeDtypeStruct((S, D), BF16),
                   jax.ShapeDtypeStruct((1, D), F32), jax.ShapeDtypeStruct((1, D), F32)],
        compiler_params=_cparams(),
    )(dh2, x1, w_pre_mlp, dx2, mixed, w_post)


def _pre_norm_bwd(dh, x, w, dx1):
    def body(dh_ref, x_ref, w_ref, dx1_ref, dx_ref, dw_ref):
        i = pl.program_id(0)
        dxa, dwt = _rms_bwd(x_ref[...], w_ref[...], dh_ref[...])
        dx_ref[...] = dx1_ref[...] + dxa

        @pl.when(i == 0)
        def _():
            dw_ref[...] = jnp.zeros_like(dw_ref)

        dw_ref[...] += jnp.sum(dwt, axis=0, keepdims=True)

    return pl.pallas_call(
        body, name="pre_norm_bwd", grid=(S // TR,),
        in_specs=[_row_spec(D), _row_spec(D), _vec_spec(D), _row_spec(D)], out_specs=[_row_spec(D), _vec_spec(D)],
        out_shape=[jax.ShapeDtypeStruct((S, D), F32), jax.ShapeDtypeStruct((1, D), F32)], compiler_params=_cparams(),
    )(dh, x, w, dx1)


BQ = 256
NQ = S // BQ
LANE_BETA, LANE_G = 8, 12


def _gate_lanes(shape):
    lane = lax.broadcasted_iota(jnp.int32, shape, 1)
    return lane < LANE_BETA, (lane >= LANE_BETA) & (lane < LANE_G), (lane >= LANE_G) & (lane < LANE_G + NGH)


def _gates(proj, bias_vec, alog_vec):
    def body(s_ref, b_ref, a_ref, o_ref, carry_ref):
        i = pl.program_id(0)

        @pl.when(i == 0)
        def _():
            carry_ref[...] = jnp.zeros_like(carry_ref)

        z = s_ref[...] + b_ref[...]
        tail = jnp.log(1.0 + jnp.exp(-jnp.abs(z)))
        sp = jnp.maximum(z, 0.0) + tail
        lf = jnp.minimum(z, 0.0) - tail
        r = lax.broadcasted_iota(jnp.int32, (BQ, BQ), 0)
        c = lax.broadcasted_iota(jnp.int32, (BQ, BQ), 1)
        tri = (c <= r).astype(F32)
        cum = _hdot(tri, lf) + carry_ref[...]
        carry_ref[...] = cum[BQ - 1:BQ, :]
        is_fox, is_beta, is_g = _gate_lanes(z.shape)
        o_ref[...] = jnp.where(is_fox, cum, jnp.where(is_beta, _sigmoid(z), jnp.where(is_g, -jnp.exp(a_ref[...]) * sp, 0.0)))

    return pl.pallas_call(
        body, name="gates", grid=(NQ,),
        in_specs=[pl.BlockSpec((BQ, LANES), lambda i: (i, BLK_SMALL)), _vec_spec(LANES), _vec_spec(LANES)],
        out_specs=pl.BlockSpec((BQ, LANES), lambda i: (i, 0)), out_shape=jax.ShapeDtypeStruct((S, LANES), F32),
        scratch_shapes=[pltpu.VMEM((1, LANES), F32)], compiler_params=_cparams(),
    )(proj, bias_vec, alog_vec)


def _gates_bwd(proj, bias_vec, alog_vec, dgates_gdn, dcum_fox, dproj):
    def body(s_ref, b_ref, a_ref, dg_ref, dc_ref, dproj_in, dproj_ref, red_ref, carry_ref):
        del dproj_in
        i = pl.program_id(0)

        @pl.when(i == 0)
        def _():
            carry_ref[...] = jnp.zeros_like(carry_ref)
            red_ref[...] = jnp.zeros_like(red_ref)

        z = s_ref[...] + b_ref[...]
        dg = dg_ref[...] + dc_ref[...]
        r = lax.broadcasted_iota(jnp.int32, (BQ, BQ), 0)
        c = lax.broadcasted_iota(jnp.int32, (BQ, BQ), 1)
        upper = (c >= r).astype(F32)
        dlf = _hdot(upper, dg) + carry_ref[...]
        carry_ref[...] = dlf[0:1, :]
        sig = _sigmoid(z)
        g_scale = -jnp.exp(a_ref[...])
        is_fox, is_beta, is_g = _gate_lanes(z.shape)
        ds = jnp.where(is_fox, dlf * (1.0 - sig), jnp.where(is_beta, dg * sig * (1.0 - sig), jnp.where(is_g, dg * g_scale * sig, 0.0)))
        dproj_ref[:, 0:LANES] = ds.astype(BF16)
        dproj_ref[:, LANES:2 * LANES] = jnp.zeros((BQ, LANES), BF16)
        dalog = jnp.where(is_g, dg * g_scale * _softplus(z), 0.0)
        sums = jnp.sum(ds, axis=0, keepdims=True)
        red_ref[0:1, :] += jnp.where(is_fox[0:1], sums, 0.0)
        red_ref[1:2, :] += pltpu.roll(jnp.where(is_g[0:1], sums, 0.0), LANES - LANE_G, 1)
        red_ref[2:3, :] += pltpu.roll(jnp.sum(dalog, axis=0, keepdims=True), LANES - LANE_G, 1)

    blk = pl.BlockSpec((BQ, LANES), lambda i: (NQ - 1 - i, 0))
    return pl.pallas_call(
        body, name="gates_bwd", grid=(NQ,),
        in_specs=[pl.BlockSpec((BQ, LANES), lambda i: (NQ - 1 - i, BLK_SMALL)), _vec_spec(LANES), _vec_spec(LANES), blk, blk,
                  pl.BlockSpec(memory_space=pl.ANY)],
        out_specs=[pl.BlockSpec((BQ, 2 * LANES), lambda i: (NQ - 1 - i, BLK_SMALL // 2)), pl.BlockSpec((8, LANES), lambda i: (0, 0))],
        out_shape=[jax.ShapeDtypeStruct((S, DPROJ_PAD), BF16), jax.ShapeDtypeStruct((8, LANES), F32)],
        input_output_aliases={5: 0},
        scratch_shapes=[pltpu.VMEM((1, LANES), F32)], compiler_params=_cparams(),
    )(proj, bias_vec, alog_vec, dgates_gdn, dcum_fox, dproj)


FOX_SCALE = FHD ** -0.5
FOX_PAIRS = 2
FOX_PAIRS_BWD = 2


def _head_mask(e):
    lane = lax.broadcasted_iota(jnp.int32, (1, LANES), 1)
    return (lane >= e * FHD) & (lane < (e + 1) * FHD)


def _lane_col(vals, index):
    lane = lax.broadcasted_iota(jnp.int32, vals.shape, 1)
    return jnp.sum(jnp.where(lane == index, vals, 0.0), axis=1, keepdims=True)


def _sublane_row(vals, index):
    row = lax.broadcasted_iota(jnp.int32, vals.shape, 0)
    return jnp.sum(jnp.where(row == index, vals, 0.0), axis=0, keepdims=True)


def _pair_cols(c0, c1):
    lane = lax.broadcasted_iota(jnp.int32, (c0.shape[0], 2), 1)
    return jnp.where(lane == 0, c0, c1)


def _split3(x):
    hi = x.astype(BF16).astype(F32)
    rest = x - hi
    mid = rest.astype(BF16).astype(F32)
    return hi, mid, (rest - mid).astype(BF16).astype(F32)


def _fox_operand(vals, e, cum, is_query):
    lane = lax.broadcasted_iota(jnp.int32, (1, LANES), 1)
    base = (1 - e) * FHD
    parts = _split3(cum)
    own = jnp.where(_head_mask(e), vals * FOX_SCALE if is_query else vals, 0.0)
    cum_at, ones_at = (base, base + 3) if is_query else (base + 3, base)
    sign = 1.0 if is_query else -1.0
    out = own + jnp.where((lane >= ones_at) & (lane < ones_at + 3), 1.0, 0.0)
    for i, part in enumerate(parts):
        out = out + jnp.where(lane == cum_at + i, sign * part, 0.0)
    return out.astype(BF16)


def _causal_block():
    return lax.broadcasted_iota(jnp.int32, (BQ, BQ), 1) <= lax.broadcasted_iota(jnp.int32, (BQ, BQ), 0)


def _head_rms(o, masks):
    o2 = o * o
    r = [lax.rsqrt(jnp.sum(jnp.where(mk, o2, 0.0), axis=1, keepdims=True) * (1.0 / FHD) + EPS) for mk in masks]
    return jnp.where(masks[0], r[0], r[1])


def _hosted(exchange):
    if exchange is None:
        return [], [], [], [], []
    return (exchange.inputs, [HBM] * len(exchange.inputs), [HBM] * len(exchange.out_shape), exchange.out_shape,
            exchange.sem_shapes())


def _fox_fwd(proj, gates, w2, exchange=None):
    ex_in, ex_in_specs, ex_out_specs, ex_out_shape, ex_scratch = _hosted(exchange)

    n_in = 3 * FOX_PAIRS + 2
    heads = [(pp, e) for pp in range(FOX_PAIRS) for e in range(2)]

    def body(*refs):
        qkv_refs, g_ref, w_ref = refs[:3 * FOX_PAIRS], refs[3 * FOX_PAIRS], refs[3 * FOX_PAIRS + 1]
        mix_ref, o_ref, lse_ref = refs[n_in + len(ex_in):n_in + 3 + len(ex_in)]
        ka_ref, vb_ref = refs[n_in + 3 + len(ex_in) + len(ex_out_shape):n_in + 5 + len(ex_in) + len(ex_out_shape)]
        ex_refs = refs[n_in:n_in + len(ex_in)] + refs[n_in + 3 + len(ex_in):n_in + 3 + len(ex_in) + len(ex_out_shape)] + refs[-2:]
        grp, qi = pl.program_id(0), pl.program_id(1)

        def head_index(pp, e):
            return 2 * (FOX_PAIRS * grp + pp) + e

        if exchange is not None:
            @pl.when((grp == 0) & (qi == 0))
            def _():
                exchange.start(*exchange.split(ex_refs))

        @pl.when(qi == 0)
        def _():
            gt = g_ref[...]
            for pp in range(FOX_PAIRS):
                kv = qkv_refs[3 * pp + 1][...]
                for e in range(2):
                    ka_ref[2 * pp + e] = _fox_operand(kv, e, _lane_col(gt, head_index(pp, e)), False)
                vb_ref[pp] = qkv_refs[3 * pp + 2][...].astype(BF16)

        masks = [_head_mask(0), _head_mask(1)]
        gt = g_ref[pl.ds(pl.multiple_of(qi * BQ, BQ), BQ), :]
        qs = [_fox_operand(qkv_refs[3 * pp][...], e, _lane_col(gt, head_index(pp, e)), True) for pp, e in heads]
        n = range(len(heads))

        def block(kj, carry, diagonal):
            rows = pl.ds(pl.multiple_of(kj * BQ, BQ), BQ)
            s = [_dot(qs[i], ka_ref[i, rows, :], 1, 1) for i in n]
            if diagonal:
                s = [jnp.where(_causal_block(), s[i], -jnp.inf) for i in n]
            m_new = [jnp.maximum(carry[i][0], jnp.max(s[i], axis=-1, keepdims=True)) for i in n]
            p = [jnp.exp(s[i] - m_new[i]) for i in n]
            alpha = [jnp.exp(carry[i][0] - m_new[i]) for i in n]
            l_new = [alpha[i] * carry[i][1] + jnp.sum(p[i], axis=-1, keepdims=True) for i in n]
            pv = [_dot(p[i], vb_ref[heads[i][0], rows, :]) for i in n]
            return tuple((m_new[i], l_new[i], alpha[i] * carry[i][2] + pv[i]) for i in n)

        one = (jnp.full((BQ, 1), -jnp.inf, F32), jnp.zeros((BQ, 1), F32), jnp.zeros((BQ, LANES), F32))
        below = lax.fori_loop(0, qi, lambda kj, carry: block(kj, carry, False), (one,) * len(heads))
        done = block(qi, below, True)
        for pp in range(FOX_PAIRS):
            (m0, l0, a0), (m1, l1, a1) = done[2 * pp], done[2 * pp + 1]
            o = jnp.where(masks[0], a0 / l0, a1 / l1)
            cols = slice(pp * LANES, (pp + 1) * LANES)
            o_ref[:, cols] = o
            mix_ref[:, cols] = (o * _head_rms(o, masks) * w_ref[...]).astype(BF16)
            lse_ref[pp] = _pair_cols(m0 + jnp.log(l0), m1 + jnp.log(l1))

        if exchange is not None:
            @pl.when((grp == NPAIR // FOX_PAIRS - 1) & (qi == NQ - 1))
            def _():
                exchange.finish(*exchange.split(ex_refs))

    qkv_specs = []
    for pp in range(FOX_PAIRS):
        qkv_specs.append(pl.BlockSpec((BQ, LANES), lambda g, i, pp=pp: (i, 3 * (FOX_PAIRS * g + pp))))
        qkv_specs.append(pl.BlockSpec((S, LANES), lambda g, i, pp=pp: (0, 3 * (FOX_PAIRS * g + pp) + 1)))
        qkv_specs.append(pl.BlockSpec((S, LANES), lambda g, i, pp=pp: (0, 3 * (FOX_PAIRS * g + pp) + 2)))
    blk = pl.BlockSpec((BQ, FOX_PAIRS * LANES), lambda g, i: (i, g))
    res = pl.pallas_call(
        body, name="fox_fwd", grid=(NPAIR // FOX_PAIRS, NQ),
        in_specs=qkv_specs + [pl.BlockSpec((S, LANES), lambda g, i: (0, 0)), pl.BlockSpec((1, LANES), lambda g, i: (0, 0))]
        + ex_in_specs,
        out_specs=[blk, blk, pl.BlockSpec((FOX_PAIRS, BQ, 2), lambda g, i: (g, i, 0))] + ex_out_specs,
        out_shape=[jax.ShapeDtypeStruct((S, D), BF16), jax.ShapeDtypeStruct((S, DFOX), F32),
                   jax.ShapeDtypeStruct((NPAIR, S, 2), F32)] + ex_out_shape,
        scratch_shapes=[pltpu.VMEM((2 * FOX_PAIRS, S, LANES), BF16), pltpu.VMEM((FOX_PAIRS, S, LANES), BF16)] + ex_scratch,
        compiler_params=_cparams(),
    )(*([proj] * (3 * FOX_PAIRS)), gates, w2, *ex_in)
    return res[0], res[1], res[2], res[3:]


def _fox_norm_bwd(o, dmix, w2, exchange=None):
    ex_in, ex_in_specs, ex_out_specs, ex_out_shape, ex_scratch = _hosted(exchange)

    def body(*refs):
        o_ref, g_ref, w_ref = refs[:3]
        do_ref, dl_ref, dw_ref = refs[3 + len(ex_in):6 + len(ex_in)]
        ex_refs = refs[3:3 + len(ex_in)] + refs[6 + len(ex_in):]
        hp, qi = pl.program_id(0), pl.program_id(1)

        if exchange is not None:
            @pl.when((hp == 0) & (qi == 0))
            def _():
                exchange.start(*exchange.split(ex_refs))

        masks = [_head_mask(0), _head_mask(1)]
        ov = o_ref[...]
        g = g_ref[...]
        r = _head_rms(ov, masks)
        gw = g * w_ref[...]
        gwo = gw * ov
        mean = [jnp.sum(jnp.where(mk, gwo, 0.0), axis=1, keepdims=True) * (1.0 / FHD) for mk in masks]
        do = r * gw - ov * (r * r * r) * jnp.where(masks[0], mean[0], mean[1])
        do_ref[...] = do.astype(BF16)
        doo = do * ov
        dl_ref[...] = _pair_cols(*[jnp.sum(jnp.where(mk, doo, 0.0), axis=1, keepdims=True) for mk in masks])

        @pl.when((hp == 0) & (qi == 0))
        def _():
            dw_ref[...] = jnp.zeros_like(dw_ref)

        dw_ref[...] += jnp.sum(g * ov * r, axis=0, keepdims=True)

        @pl.when((hp == NPAIR - 1) & (qi == NQ - 1))
        def _():
            dw = dw_ref[...]
            dw_ref[...] = dw + pltpu.roll(dw, FHD, 1)
            if exchange is not None:
                exchange.finish(*exchange.split(ex_refs))

    blk = pl.BlockSpec((BQ, LANES), lambda hp, i: (i, hp))
    vec = pl.BlockSpec((1, LANES), lambda hp, i: (0, 0))
    res = pl.pallas_call(
        body, name="fox_norm_bwd", grid=(NPAIR, NQ), in_specs=[blk, blk, vec] + ex_in_specs,
        out_specs=[blk, pl.BlockSpec((None, BQ, 2), lambda hp, i: (hp, i, 0)), vec] + ex_out_specs,
        out_shape=[jax.ShapeDtypeStruct((S, DFOX), BF16), jax.ShapeDtypeStruct((NPAIR, S, 2), F32),
                   jax.ShapeDtypeStruct((1, LANES), F32)] + ex_out_shape,
        scratch_shapes=ex_scratch, compiler_params=_cparams(),
    )(o, dmix, w2, *ex_in)
    return res[0], res[1], res[2], res[3:]


def _fox_bwd(proj, do, gates, lse, delta, exchange=None):
    ex_in, ex_in_specs, ex_out_specs, ex_out_shape, ex_scratch = _hosted(exchange)

    pg = FOX_PAIRS_BWD
    n_in = 3 * pg + 4
    heads = [(pp, e) for pp in range(pg) for e in range(2)]

    def body(*refs):
        qkv_refs = refs[:3 * pg]
        do_ref, g_ref, lse_ref, dl_ref = refs[3 * pg:n_in]
        dproj_ref, dc_ref = refs[n_in + len(ex_in):n_in + 2 + len(ex_in)]
        qa_ref, dq_ref = refs[n_in + 2 + len(ex_in) + len(ex_out_shape):n_in + 4 + len(ex_in) + len(ex_out_shape)]
        ex_refs = refs[n_in:n_in + len(ex_in)] + refs[n_in + 2 + len(ex_in):n_in + 2 + len(ex_in) + len(ex_out_shape)] + refs[-2:]
        grp, kj = pl.program_id(0), pl.program_id(1)

        def head_index(pp, e):
            return 2 * (pg * grp + pp) + e

        if exchange is not None:
            @pl.when((grp == 0) & (kj == 0))
            def _():
                exchange.start(*exchange.split(ex_refs))

        @pl.when(kj == 0)
        def _():
            gt = g_ref[...]
            for pp in range(pg):
                qv = qkv_refs[3 * pp][...]
                for e in range(2):
                    qa_ref[2 * pp + e] = _fox_operand(qv, e, _lane_col(gt, head_index(pp, e)), True)
            dq_ref[...] = jnp.zeros_like(dq_ref)

        @pl.when((grp == 0) & (kj == 0))
        def _():
            dc_ref[...] = jnp.zeros_like(dc_ref)

        masks = [_head_mask(0), _head_mask(1)]
        krows = pl.ds(pl.multiple_of(kj * BQ, BQ), BQ)
        gk = g_ref[krows, :]
        kas = [_fox_operand(qkv_refs[3 * pp + 1][...], e, _lane_col(gk, head_index(pp, e)), False) for pp, e in heads]
        vbs = [qkv_refs[3 * pp + 2][...].astype(BF16) for pp in range(pg)]
        lane = lax.broadcasted_iota(jnp.int32, (BQ, LANES), 1)
        n = range(len(heads))

        def block(qi, carry, diagonal):
            dks, dvs, css = carry
            rows = pl.ds(pl.multiple_of(qi * BQ, BQ), BQ)
            qa = [qa_ref[i, rows, :] for i in n]
            s = [_dot(qa[i], kas[i], 1, 1) for i in n]
            if diagonal:
                s = [jnp.where(_causal_block(), s[i], -jnp.inf) for i in n]
            dov = [do_ref[rows, pp * LANES:(pp + 1) * LANES] for pp in range(pg)]
            doe = [jnp.where(masks[e], dov[pp], jnp.zeros_like(dov[pp])) for pp, e in heads]
            lse2 = [lse_ref[pp, rows, :] for pp in range(pg)]
            dl2 = [dl_ref[pp, rows, :] for pp in range(pg)]
            p = [jnp.exp(s[i] - _lane_col(lse2[heads[i][0]], heads[i][1])) for i in n]
            dp = [_dot(doe[i], vbs[heads[i][0]], 1, 1) for i in n]
            ds = [p[i] * (dp[i] - _lane_col(dl2[heads[i][0]], heads[i][1])) for i in n]
            dv_part = [_dot(p[i], doe[i], 0, 0) for i in n]
            dk_part = [_dot(ds[i], jnp.where(masks[heads[i][1]], qa[i], jnp.zeros_like(qa[i])), 0, 0) for i in n]
            dq_part = [jnp.where(masks[heads[i][1]], _dot(ds[i], kas[i]), 0.0) for i in n]
            css = tuple(css[i] + jnp.sum(ds[i], axis=0, keepdims=True) for i in n)
            dc = jnp.zeros((BQ, LANES), F32)
            for i in n:
                dc = dc + jnp.where(lane == head_index(*heads[i]), jnp.sum(ds[i], axis=1, keepdims=True), 0.0)
            for pp in range(pg):
                dq_ref[pp, rows, :] += (dq_part[2 * pp] + dq_part[2 * pp + 1]) * FOX_SCALE
            dc_ref[rows, :] += dc
            dks = tuple(dks[pp] + dk_part[2 * pp] + dk_part[2 * pp + 1] for pp in range(pg))
            dvs = tuple(dvs[pp] + dv_part[2 * pp] + dv_part[2 * pp + 1] for pp in range(pg))
            return dks, dvs, css

        zero = jnp.zeros((BQ, LANES), F32)
        first = block(kj, ((zero,) * pg, (zero,) * pg, (jnp.zeros((1, BQ), F32),) * len(heads)), True)
        dks, dvs, css = lax.fori_loop(kj + 1, NQ, lambda qi, carry: block(qi, carry, False), first)
        r = lax.broadcasted_iota(jnp.int32, (BQ, BQ), 0)
        c = lax.broadcasted_iota(jnp.int32, (BQ, BQ), 1)
        dcol = jnp.zeros((BQ, LANES), F32)
        for i in n:
            col = jnp.sum(jnp.where(r == c, css[i], 0.0), axis=1, keepdims=True)
            dcol = dcol + jnp.where(lane == head_index(*heads[i]), col, 0.0)
        dc_ref[krows, :] -= dcol
        for pp in range(pg):
            base = 3 * pp * LANES
            dproj_ref[krows, base + LANES:base + 2 * LANES] = dks[pp].astype(BF16)
            dproj_ref[krows, base + 2 * LANES:base + 3 * LANES] = dvs[pp].astype(BF16)

        @pl.when(kj == NQ - 1)
        def _():
            for pp in range(pg):
                dproj_ref[:, 3 * pp * LANES:(3 * pp + 1) * LANES] = dq_ref[pp].astype(BF16)

        if exchange is not None:
            @pl.when((grp == NPAIR // pg - 1) & (kj == NQ - 1))
            def _():
                exchange.finish(*exchange.split(ex_refs))

    qkv_specs = []
    for pp in range(pg):
        qkv_specs.append(pl.BlockSpec((S, LANES), lambda g, j, pp=pp: (0, 3 * (pg * g + pp))))
        qkv_specs.append(pl.BlockSpec((BQ, LANES), lambda g, j, pp=pp: (j, 3 * (pg * g + pp) + 1)))
        qkv_specs.append(pl.BlockSpec((BQ, LANES), lambda g, j, pp=pp: (j, 3 * (pg * g + pp) + 2)))
    pair = pl.BlockSpec((pg, S, 2), lambda g, j: (g, 0, 0))
    res = pl.pallas_call(
        body, name="fox_bwd", grid=(NPAIR // pg, NQ),
        in_specs=qkv_specs + [pl.BlockSpec((S, pg * LANES), lambda g, j: (0, g)), pl.BlockSpec((S, LANES), lambda g, j: (0, 0)),
                              pair, pair] + ex_in_specs,
        out_specs=[pl.BlockSpec((S, 3 * pg * LANES), lambda g, j: (0, g)), pl.BlockSpec((S, LANES), lambda g, j: (0, 0))]
        + ex_out_specs,
        out_shape=[jax.ShapeDtypeStruct((S, DPROJ_PAD), BF16), jax.ShapeDtypeStruct((S, LANES), F32)] + ex_out_shape,
        scratch_shapes=[pltpu.VMEM((2 * pg, S, LANES), BF16), pltpu.VMEM((pg, S, LANES), F32)] + ex_scratch,
        compiler_params=_cparams(),
    )(*([proj] * (3 * pg)), do, gates, lse, delta, *ex_in)
    return res[0], res[1], res[2:]


NQKV = 3 * NGH
GDN_QSCALE = GHD ** -0.5


def _shift_down(x, s):
    if s == 0:
        return x
    row = lax.broadcasted_iota(jnp.int32, x.shape, 0)
    return jnp.where(row >= s, pltpu.roll(x, s, 0), 0.0)


def _shift_up(x, s):
    if s == 0:
        return x
    n = x.shape[0]
    row = lax.broadcasted_iota(jnp.int32, x.shape, 0)
    return jnp.where(row < n - s, pltpu.roll(x, n - s, 0), 0.0)


def _conv_pre(xv, wv):
    pre = xv * wv[CONV_K - 1:CONV_K, :]
    for j in range(CONV_K - 1):
        pre = pre + _shift_down(xv, CONV_K - 1 - j) * wv[j:j + 1, :]
    return pre


def _l2_factors(b):
    return b < 2 * NGH, jnp.where(b < NGH, GDN_QSCALE, 1.0)


def _gdn_pre(proj, conv_w):
    def body(x_ref, w_ref, o_ref):
        b = pl.program_id(0)
        c = _silu(_conv_pre(x_ref[...], w_ref[...]))
        normed, scale = _l2_factors(b)
        rs = lax.rsqrt(jnp.sum(c * c, axis=-1, keepdims=True) + EPS)
        o_ref[...] = c * jnp.where(normed, rs, 1.0) * scale

    return pl.pallas_call(
        body, name="gdn_pre", grid=(NQKV,),
        in_specs=[pl.BlockSpec((S, GHD), lambda b: (0, BLK_GDN + b)), pl.BlockSpec((CONV_K, GHD), lambda b: (0, b))],
        out_specs=pl.BlockSpec((S, GHD), lambda b: (0, b)),
        out_shape=jax.ShapeDtypeStruct((S, NQKV * GHD), F32), compiler_params=_cparams(),
    )(proj, conv_w)


def _gdn_pre_bwd(proj, conv_w, dqkv, dproj):
    def body(x_ref, w_ref, dy_ref, dproj_in, dx_ref, dw_ref):
        del dproj_in
        b = pl.program_id(0)
        xv = x_ref[...]
        wv = w_ref[...]
        pre = _conv_pre(xv, wv)
        sig = _sigmoid(pre)
        c = pre * sig
        normed, scale = _l2_factors(b)
        g = dy_ref[...] * scale
        rs = lax.rsqrt(jnp.sum(c * c, axis=-1, keepdims=True) + EPS)
        dc_n = rs * g - c * (rs * rs * rs) * jnp.sum(g * c, axis=-1, keepdims=True)
        dc = jnp.where(normed, dc_n, g)
        dpre = dc * sig * (1.0 + pre * (1.0 - sig))
        dx = dpre * wv[CONV_K - 1:CONV_K, :]
        for j in range(CONV_K - 1):
            dx = dx + _shift_up(dpre, CONV_K - 1 - j) * wv[j:j + 1, :]
        dx_ref[...] = dx.astype(BF16)
        for j in range(CONV_K):
            dw_ref[j:j + 1, :] = jnp.sum(dpre * _shift_down(xv, CONV_K - 1 - j), axis=0, keepdims=True)

    return pl.pallas_call(
        body, name="gdn_pre_bwd", grid=(NQKV,),
        in_specs=[pl.BlockSpec((S, GHD), lambda b: (0, BLK_GDN + b)), pl.BlockSpec((CONV_K, GHD), lambda b: (0, b)),
                  pl.BlockSpec((None, S, GHD), lambda b: (b // NGH, 0, b % NGH)), pl.BlockSpec(memory_space=pl.ANY)],
        out_specs=[pl.BlockSpec((S, GHD), lambda b: (0, BLK_GDN + b)), pl.BlockSpec((CONV_K, GHD), lambda b: (0, b))],
        out_shape=[jax.ShapeDtypeStruct((S, DPROJ_PAD), BF16), jax.ShapeDtypeStruct((CONV_K, NQKV * GHD), F32)],
        input_output_aliases={3: 0}, compiler_params=_cparams(),
    )(proj, conv_w, dqkv, dproj)


CB = 4
NCB = NCH // CB


def _chunk_prep(qs, ks, vs, gcols, bcols, t_saved=None):
    n = range(len(qs))
    r = lax.broadcasted_iota(jnp.int32, (CHUNK, CHUNK), 0)
    c = lax.broadcasted_iota(jnp.int32, (CHUNK, CHUNK), 1)
    incl = c <= r
    eye = (r == c).astype(F32)
    grow = [jnp.sum(gcols[i] * eye, axis=0, keepdims=True) for i in n]
    gc_col = [jnp.sum(jnp.where(incl, grow[i], 0.0), axis=1, keepdims=True) for i in n]
    gc_row = [jnp.sum(jnp.where(r <= c, gcols[i], 0.0), axis=0, keepdims=True) for i in n]
    decay = [jnp.exp(jnp.where(incl, gc_col[i] - gc_row[i], -jnp.inf)) for i in n]
    kb = [ks[i] * bcols[i] for i in n]
    vb = [vs[i] * bcols[i] for i in n]
    kk = [_mm_nt(kb[i], ks[i]) for i in n]
    m = [jnp.where(c < r, kk[i] * decay[i], 0.0) for i in n]
    if t_saved is None:
        t_inv = [eye - m[i] for i in n]
        p = [_dot3(m[i], m[i]) for i in n]
        for step in range(5):
            t_inv = [t_inv[i] + _dot3(t_inv[i], p[i]) for i in n]
            if step < 4:
                p = [_dot3(p[i], p[i]) for i in n]
    else:
        t_inv = [_saved_inverse(m[i], t_saved[i]) for i in n]
    egc = [jnp.exp(gc_col[i]) for i in n]
    u = [_mm_nn(t_inv[i], vb[i]) for i in n]
    w = [_mm_nn(t_inv[i], kb[i] * egc[i]) for i in n]
    qk = [_mm_nt(qs[i], ks[i]) for i in n]
    gc_last = [gc_col[i][CHUNK - 1:CHUNK, :] for i in n]
    return [(u[i], w[i], qk[i] * decay[i], qs[i] * egc[i], ks[i] * jnp.exp(gc_last[i] - gc_col[i]), jnp.exp(gc_last[i]),
             t_inv[i]) for i in n]


def _prep_specs():
    rows = CB * CHUNK
    qs = pl.BlockSpec((rows, GHD), lambda i, h: (i, h))
    ks = pl.BlockSpec((rows, GHD), lambda i, h: (i, NGH + h))
    vs = pl.BlockSpec((rows, GHD), lambda i, h: (i, 2 * NGH + h))
    gs = pl.BlockSpec((rows, LANES), lambda i, h: (i, 0))
    a_s = pl.BlockSpec((None, rows, CHUNK), lambda i, h: (h, i, 0))
    gl_s = pl.BlockSpec((None, CB, 1, LANES), lambda i, h: (h, i, 0, 0))
    return qs, ks, vs, gs, a_s, gl_s


def _gdn_prep(qkv, gates, exchange=None):
    ex_in, ex_in_specs, ex_out_specs, ex_out_shape, ex_scratch = _hosted(exchange)

    def body(*refs):
        q_ref, k_ref, v_ref, g_ref = refs[:4]
        u_ref, w_ref, qd_ref, kd_ref, a_ref, gl_ref, t_ref = refs[4 + len(ex_in):11 + len(ex_in)]
        ex_refs = refs[4:4 + len(ex_in)] + refs[11 + len(ex_in):]
        h = pl.program_id(1)

        if exchange is not None:
            @pl.when((pl.program_id(0) == 0) & (h == 0))
            def _():
                exchange.start(*exchange.split(ex_refs))

        chunks = [pl.ds(cidx * CHUNK, CHUNK) for cidx in range(CB)]
        gts = [g_ref[rows, :] for rows in chunks]
        outs = _chunk_prep([q_ref[rows, :] for rows in chunks], [k_ref[rows, :] for rows in chunks],
                           [v_ref[rows, :] for rows in chunks], [_lane_col(gt, LANE_G + h) for gt in gts],
                           [_lane_col(gt, LANE_BETA + h) for gt in gts])
        for cidx, rows in enumerate(chunks):
            u, w, a, qd, kd, gl, t_inv = outs[cidx]
            u_ref[rows, :] = u
            w_ref[rows, :] = w
            qd_ref[rows, :] = qd
            kd_ref[rows, :] = kd
            a_ref[rows, :] = a
            t_ref[rows, :] = t_inv
            gl_ref[cidx] = jnp.broadcast_to(gl, (1, LANES))

        if exchange is not None:
            @pl.when((pl.program_id(0) == NCB - 1) & (h == NGH - 1))
            def _():
                exchange.finish(*exchange.split(ex_refs))

    qs, ks, vs, gs, a_s, gl_s = _prep_specs()
    tok = jax.ShapeDtypeStruct((S, DGDN), F32)
    sq = jax.ShapeDtypeStruct((NGH, S, CHUNK), F32)
    res = pl.pallas_call(
        body, name="gdn_prep", grid=(NCB, NGH), in_specs=[qs, ks, vs, gs] + ex_in_specs,
        out_specs=[qs, qs, qs, qs, a_s, gl_s, a_s] + ex_out_specs,
        out_shape=[tok, tok, tok, tok, sq, jax.ShapeDtypeStruct((NGH, NCH, 1, LANES), F32), sq] + ex_out_shape,
        scratch_shapes=ex_scratch, compiler_params=_cparams(),
    )(qkv, qkv, qkv, gates, *ex_in)
    return res[:7], res[7:]


def _gdn_prep_bwd(qkv, gates, t_inv, du, dw, dqd, dkd, da, dgl):
    def body(q_ref, k_ref, v_ref, g_ref, t_ref, du_ref, dw_ref, dqd_ref, dkd_ref, da_ref, dgl_ref, dqkv_ref, dg_ref):
        h = pl.program_id(1)

        @pl.when(h == 0)
        def _():
            dg_ref[...] = jnp.zeros_like(dg_ref)

        lane = lax.broadcasted_iota(jnp.int32, (CHUNK, LANES), 1)
        chunks = [pl.ds(cidx * CHUNK, CHUNK) for cidx in range(CB)]
        gts = [g_ref[rows, :] for rows in chunks]
        t_saved = [t_ref[rows, :] for rows in chunks]
        _, vjp = jax.vjp(lambda *args: [o[:6] for o in _chunk_prep(*args, t_saved=t_saved)],
                         [q_ref[rows, :] for rows in chunks], [k_ref[rows, :] for rows in chunks],
                         [v_ref[rows, :] for rows in chunks], [_lane_col(gt, LANE_G + h) for gt in gts],
                         [_lane_col(gt, LANE_BETA + h) for gt in gts])
        dqs, dks, dvs, dgcs, dbcs = vjp([(du_ref[rows, :], dw_ref[rows, :], da_ref[rows, :], dqd_ref[rows, :],
                                          dkd_ref[rows, :], dgl_ref[cidx][:, 0:1]) for cidx, rows in enumerate(chunks)])
        for cidx, rows in enumerate(chunks):
            dq, dk, dv, dgc, dbc = dqs[cidx], dks[cidx], dvs[cidx], dgcs[cidx], dbcs[cidx]
            dqkv_ref[0, rows, :] = dq
            dqkv_ref[1, rows, :] = dk
            dqkv_ref[2, rows, :] = dv
            dg_ref[rows, :] += jnp.where(lane == LANE_G + h, dgc, 0.0) + jnp.where(lane == LANE_BETA + h, dbc, 0.0)

    qs, ks, vs, gs, a_s, gl_s = _prep_specs()
    return pl.pallas_call(
        body, name="gdn_prep_bwd", grid=(NCB, NGH), in_specs=[qs, ks, vs, gs, a_s, qs, qs, qs, qs, a_s, gl_s],
        out_specs=[pl.BlockSpec((3, CB * CHUNK, GHD), lambda i, h: (0, i, h)), gs],
        out_shape=[jax.ShapeDtypeStruct((3, S, DGDN), F32), jax.ShapeDtypeStruct((S, LANES), F32)],
        compiler_params=_cparams(),
    )(qkv, qkv, qkv, gates, t_inv, du, dw, dqd, dkd, da, dgl)


SCAN_HEADS = 2


def _scan_specs():
    wide = SCAN_HEADS * GHD
    hs = pl.BlockSpec((S, wide), lambda g: (0, g))
    a_s = pl.BlockSpec((SCAN_HEADS, S, CHUNK), lambda g: (g, 0, 0))
    gl_s = pl.BlockSpec((SCAN_HEADS, NCH, 1, LANES), lambda g: (g, 0, 0, 0))
    st_s = pl.BlockSpec((SCAN_HEADS, NCH, GHD, GHD), lambda g: (g, 0, 0, 0))
    gz_s = pl.BlockSpec((S, wide), lambda g: (0, BLK_GZ // SCAN_HEADS + g))
    mix_s = pl.BlockSpec((S, wide), lambda g: (0, NPAIR // SCAN_HEADS + g))
    return hs, a_s, gl_s, st_s, gz_s, mix_s


def _head_cols(hh):
    return slice(hh * GHD, (hh + 1) * GHD)


def _gdn_scan(u, w, qd, kd, a, gl, proj, w_norm, mix):
    heads = range(SCAN_HEADS)

    def body(u_ref, w_ref, qd_ref, kd_ref, a_ref, gl_ref, z_ref, wn_ref, mix_in, mix_ref, o_ref, st_ref):
        del mix_in

        def step(ci, states):
            rows = pl.ds(pl.multiple_of(ci * CHUNK, CHUNK), CHUNK)
            for hh in heads:
                st_ref[hh, ci] = states[hh]
            ws = [_dot(w_ref[rows, _head_cols(hh)], states[hh]) for hh in heads]
            qs = [_dot(qd_ref[rows, _head_cols(hh)], states[hh]) for hh in heads]
            vn = [u_ref[rows, _head_cols(hh)] - ws[hh] for hh in heads]
            av = [_dot(a_ref[hh, rows, :], vn[hh]) for hh in heads]
            kv = [_dot(kd_ref[rows, _head_cols(hh)], vn[hh], 0, 0) for hh in heads]
            for hh in heads:
                o_ref[rows, _head_cols(hh)] = qs[hh] + av[hh]
            return tuple(states[hh] * gl_ref[hh, ci] + kv[hh] for hh in heads)

        lax.fori_loop(0, NCH, step, (jnp.zeros((GHD, GHD), F32),) * SCAN_HEADS)
        for hh in heads:
            ov = o_ref[:, _head_cols(hh)]
            mix_ref[:, _head_cols(hh)] = (ov * _rms_scale(ov) * wn_ref[...] * _silu(z_ref[:, _head_cols(hh)])).astype(BF16)

    hs, a_s, gl_s, st_s, gz_s, mix_s = _scan_specs()
    return pl.pallas_call(
        body, name="gdn_scan", grid=(NGH // SCAN_HEADS,),
        in_specs=[hs, hs, hs, hs, a_s, gl_s, gz_s, pl.BlockSpec((1, GHD), lambda h: (0, 0)), pl.BlockSpec(memory_space=pl.ANY)],
        out_specs=[mix_s, hs, st_s],
        out_shape=[jax.ShapeDtypeStruct((S, D), BF16), jax.ShapeDtypeStruct((S, DGDN), F32),
                   jax.ShapeDtypeStruct((NGH, NCH, GHD, GHD), F32)],
        input_output_aliases={8: 0}, compiler_params=_cparams(),
    )(u, w, qd, kd, a, gl, proj, w_norm, mix)


def _gdn_scan_bwd(dmix, o, proj, w_norm, u, w, qd, kd, a, gl, states, dproj):
    def body(dy_ref, o_ref, z_ref, wn_ref, u_ref, w_ref, qd_ref, kd_ref, a_ref, gl_ref, st_ref, dproj_in,
             dz_ref, du_ref, dw_ref, dqd_ref, dkd_ref, da_ref, dgl_ref, dwn_ref, do_ref):
        del dproj_in
        heads = range(SCAN_HEADS)

        @pl.when(pl.program_id(0) == 0)
        def _():
            dwn_ref[...] = jnp.zeros_like(dwn_ref)

        wn = wn_ref[...]
        for hh in heads:
            c = _head_cols(hh)
            ov = o_ref[:, c]
            zv = z_ref[:, c]
            g = dy_ref[:, c]
            sig = _sigmoid(zv)
            dz_ref[:, c] = (g * (ov * _rms_scale(ov) * wn) * sig * (1.0 + zv * (1.0 - sig))).astype(BF16)
            do, dwt = _rms_bwd(ov, wn, g * zv * sig)
            do_ref[:, c] = do
            dwn_ref[...] += jnp.sum(dwt, axis=0, keepdims=True)

        def step(t, dstates):
            ci = NCH - 1 - t
            rows = pl.ds(pl.multiple_of(ci * CHUNK, CHUNK), CHUNK)
            cols = [_head_cols(hh) for hh in heads]
            state = [st_ref[hh, ci] for hh in heads]
            dov = [do_ref[rows, cols[hh]] for hh in heads]
            wv = [w_ref[rows, cols[hh]] for hh in heads]
            ws = [_dot(wv[hh], state[hh]) for hh in heads]
            adov = [_dot(a_ref[hh, rows, :], dov[hh], 0, 0) for hh in heads]
            kds = [_dot(kd_ref[rows, cols[hh]], dstates[hh]) for hh in heads]
            dqd = [_dot(dov[hh], state[hh], 1, 1) for hh in heads]
            qdo = [_dot(qd_ref[rows, cols[hh]], dov[hh], 0, 0) for hh in heads]
            vn = [u_ref[rows, cols[hh]] - ws[hh] for hh in heads]
            dvn = [adov[hh] + kds[hh] for hh in heads]
            da = [_dot(dov[hh], vn[hh], 1, 1) for hh in heads]
            dkd = [_dot(vn[hh], dstates[hh], 1, 1) for hh in heads]
            dwv = [_dot(dvn[hh], state[hh], 1, 1) for hh in heads]
            wdv = [_dot(wv[hh], dvn[hh], 0, 0) for hh in heads]
            for hh in heads:
                da_ref[hh, rows, :] = da[hh]
                dqd_ref[rows, cols[hh]] = dqd[hh]
                dkd_ref[rows, cols[hh]] = dkd[hh]
                dgl = jnp.sum(jnp.sum(dstates[hh] * state[hh], axis=1, keepdims=True), axis=0, keepdims=True)
                dgl_ref[hh, ci] = jnp.broadcast_to(dgl, (1, LANES))
                du_ref[rows, cols[hh]] = dvn[hh]
                dw_ref[rows, cols[hh]] = -dwv[hh]
            return tuple(dstates[hh] * gl_ref[hh, ci] + qdo[hh] - wdv[hh] for hh in heads)

        lax.fori_loop(0, NCH, step, (jnp.zeros((GHD, GHD), F32),) * SCAN_HEADS)

    hs, a_s, gl_s, st_s, gz_s, mix_s = _scan_specs()
    once = pl.Buffered(buffer_count=1)
    hs1, a1, st1 = [pl.BlockSpec(s.block_shape, s.index_map, pipeline_mode=once) for s in (hs, a_s, st_s)]
    vec = pl.BlockSpec((1, GHD), lambda h: (0, 0))
    tok = jax.ShapeDtypeStruct((S, DGDN), F32)
    return pl.pallas_call(
        body, name="gdn_scan_bwd", grid=(NGH // SCAN_HEADS,),
        in_specs=[mix_s, hs, gz_s, vec, hs, hs, hs, hs, a1, gl_s, st1, pl.BlockSpec(memory_space=pl.ANY)],
        out_specs=[gz_s, hs1, hs1, hs1, hs1, a1, gl_s, vec],
        out_shape=[jax.ShapeDtypeStruct((S, DPROJ_PAD), BF16), tok, tok, tok, tok,
                   jax.ShapeDtypeStruct((NGH, S, CHUNK), F32), jax.ShapeDtypeStruct((NGH, NCH, 1, LANES), F32),
                   jax.ShapeDtypeStruct((1, GHD), F32)],
        input_output_aliases={11: 0}, scratch_shapes=[pltpu.VMEM((S, SCAN_HEADS * GHD), F32)], compiler_params=_cparams(),
    )(dmix, o, proj, w_norm, u, w, qd, kd, a, gl, states, dproj)


def _place():
    return lax.axis_index("x"), lax.axis_index("y"), lax.axis_index("c")


def _other_chips(x, y):
    return [(1 - x, y), (x, 1 - y), (1 - x, 1 - y)]


HBM = pl.BlockSpec(memory_space=pltpu.HBM)
VMEM = pl.BlockSpec(memory_space=pltpu.VMEM)


def _half_rows(ref_or_rows, half):
    rows = ref_or_rows // 2
    return pl.ds(pl.multiple_of(half * rows, rows), rows)


class _Exchange:
    def __init__(self, inputs, out_shape, n_sems, start, finish):
        self.inputs, self.out_shape, self.n_sems, self.start, self.finish = inputs, out_shape, n_sems, start, finish

    def sem_shapes(self):
        return [pltpu.SemaphoreType.DMA((self.n_sems,)), pltpu.SemaphoreType.DMA((self.n_sems,))]

    def split(self, refs):
        n_in, n_out = len(self.inputs), len(self.out_shape)
        return refs[:n_in], refs[n_in:n_in + n_out], refs[n_in + n_out], refs[n_in + n_out + 1]


def _run_exchange(ex, name):
    def body(*refs):
        parts = ex.split(refs)
        ex.start(*parts)
        ex.finish(*parts)

    return pl.pallas_call(
        body, name=name, in_specs=[HBM] * len(ex.inputs), out_specs=[HBM] * len(ex.out_shape), out_shape=ex.out_shape,
        scratch_shapes=ex.sem_shapes(), compiler_params=_cparams(),
    )(*ex.inputs)


def _allgather_exchange(shards, whole=()):
    n, nw = len(shards), len(whole)

    def plan(src, outs, send_sems, recv_sems):
        x, y, c = _place()
        chips = _other_chips(x, y)
        chip_ids = [2 * ch[0] + ch[1] for ch in chips]

        def copy(a, k, chip_index, half, to, from_src):
            rows = _half_rows(src[a].shape[0], half)
            dst = outs[a].at[chip_index, rows]
            return pltpu.make_async_remote_copy(
                src_ref=src[a].at[rows] if from_src else dst, dst_ref=dst, send_sem=send_sems.at[6 * a + k],
                recv_sem=recv_sems.at[6 * a + k], device_id=to, device_id_type=MESH)

        def whole_copy(b, k, chip_index, to):
            return pltpu.make_async_remote_copy(
                src_ref=src[n + b], dst_ref=outs[n + b].at[chip_index], send_sem=send_sems.at[6 * n + 3 * b + k],
                recv_sem=recv_sems.at[6 * n + 3 * b + k], device_id=to, device_id_type=MESH)

        me, sibling = (x, y, c), (x, y, 1 - c)
        first = [copy(a, j, 2 * x + y, c, (*chips[j], c), True) for a in range(n) for j in range(3)]
        first += [whole_copy(b, j, 2 * x + y, (*chips[j], c)) for b in range(nw) for j in range(3)]
        landing = [copy(a, j, chip_ids[j], c, me, False) for a in range(n) for j in range(3)]
        passed = [copy(a, 3 + j, chip_ids[j], c, sibling, False) for a in range(n) for j in range(3)]
        arriving = [copy(a, 3 + j, chip_ids[j], 1 - c, me, False) for a in range(n) for j in range(3)]
        arriving += [whole_copy(b, j, chip_ids[j], me) for b in range(nw) for j in range(3)]
        return first, landing, passed, arriving

    def start(*refs):
        for cp in plan(*refs)[0]:
            cp.start()

    def finish(*refs):
        first, landing, passed, arriving = plan(*refs)
        for lands, onward in zip(landing, passed):
            lands.wait_recv()
            onward.start()
        for cp in arriving:
            cp.wait_recv()
        for cp in first + passed:
            cp.wait_send()

    out_shape = [jax.ShapeDtypeStruct((NCHIP,) + s.shape, s.dtype) for s in list(shards) + list(whole)]
    return _Exchange(list(shards) + list(whole), out_shape, 6 * n + 3 * nw, start, finish)


def _with_own(gathered, own):
    x, y, _ = _place()
    return lax.dynamic_update_index_in_dim(gathered, own, 2 * x + y, axis=0)


def _simple_exchange(inputs, out_shape, copies_of):
    def start(*refs):
        for cp in copies_of(*refs):
            cp.start()

    def finish(*refs):
        for cp in copies_of(*refs):
            cp.wait()

    return _Exchange(list(inputs), out_shape, len(out_shape) * 3, start, finish)


def _pair_exchange(grads):
    def copies_of(src, outs, send_sems, recv_sems):
        x, y, c = _place()
        return [pltpu.make_async_remote_copy(
            src_ref=src[a].at[:, _half_rows(src[a].shape[1], 1 - c)], dst_ref=outs[a], send_sem=send_sems.at[a],
            recv_sem=recv_sems.at[a], device_id=(x, y, 1 - c), device_id_type=MESH) for a in range(len(src))]

    return _simple_exchange(
        grads, [jax.ShapeDtypeStruct((g.shape[0], g.shape[1] // 2, g.shape[2]), g.dtype) for g in grads], copies_of)


def _pair_sum(grads, theirs, name):
    n = len(grads)

    def body(*refs):
        south = lax.axis_index("c") == 0
        for a in range(n):
            g = refs[a][...]
            half = g.shape[0] // 2
            mine = jnp.where(south, g[:half], g[half:])
            refs[2 * n + a][...] = (mine.astype(F32) + refs[n + a][...].astype(F32)).astype(BF16)

    def specs(arrs):
        return [pl.BlockSpec((None,) + g.shape[1:], lambda j: (j, 0, 0)) for g in arrs]

    return pl.pallas_call(
        body, name=name, grid=(NCHIP,), in_specs=specs(grads) + specs(theirs), out_specs=specs(theirs),
        out_shape=[jax.ShapeDtypeStruct(g.shape, BF16) for g in theirs], compiler_params=_cparams(),
    )(*grads, *theirs)


def _chip_exchange(parts):
    def copies_of(src, outs, send_sems, recv_sems):
        x, y, c = _place()
        return [pltpu.make_async_remote_copy(
            src_ref=src[a].at[2 * chip[0] + chip[1]], dst_ref=outs[a].at[k], send_sem=send_sems.at[3 * a + k],
            recv_sem=recv_sems.at[3 * a + k], device_id=(*chip, c), device_id_type=MESH)
            for a in range(len(src)) for k, chip in enumerate(_other_chips(x, y))]

    return _simple_exchange(parts, [jax.ShapeDtypeStruct((NCHIP - 1,) + p.shape[1:], p.dtype) for p in parts], copies_of)


def _chip_sum(parts, received):
    n = len(parts)
    steps = 4

    def body(*refs):
        chip = 2 * lax.axis_index("x") + lax.axis_index("y")
        for a in range(n):
            p, r = refs[a], refs[n + a]
            own = jnp.where(chip == 0, p[0], jnp.where(chip == 1, p[1], jnp.where(chip == 2, p[2], p[3])))
            refs[2 * n + a][...] = ((own.astype(F32) + r[0].astype(F32)) + r[1].astype(F32)) + r[2].astype(F32)

    def specs(arrs):
        return [pl.BlockSpec((g.shape[0], g.shape[1] // steps, g.shape[2]), lambda i: (0, i, 0)) for g in arrs]

    out_specs = [pl.BlockSpec((g.shape[1] // steps, g.shape[2]), lambda i: (i, 0)) for g in parts]
    return pl.pallas_call(
        body, name="grads_chip_sum", grid=(steps,), in_specs=specs(parts) + specs(received), out_specs=out_specs,
        out_shape=[jax.ShapeDtypeStruct(g.shape[1:], F32) for g in parts], compiler_params=_cparams(),
    )(*parts, *received)


def _pair_share(halves):
    def copies_of(src, outs, send_sems, recv_sems):
        x, y, c = _place()
        return [pltpu.make_async_remote_copy(
            src_ref=src[a], dst_ref=outs[a], send_sem=send_sems.at[a], recv_sem=recv_sems.at[a],
            device_id=(x, y, 1 - c), device_id_type=MESH) for a in range(len(src))]

    return _simple_exchange(halves, [jax.ShapeDtypeStruct(h.shape, F32) for h in halves], copies_of)


def _adamw_math(w, g, m, v):
    nm = ADAM_B1 * m + (1.0 - ADAM_B1) * g
    nv = ADAM_B2 * v + (1.0 - ADAM_B2) * jnp.square(g)
    m_hat = nm / (1.0 - ADAM_B1 ** ADAM_STEP)
    v_hat = nv / (1.0 - ADAM_B2 ** ADAM_STEP)
    return -ADAM_LR * (m_hat / (jnp.sqrt(v_hat) + ADAM_EPS) + ADAM_WD * w), nm, nv


def _adamw_big(ws, g_mine, g_theirs, ms, vs):
    n = len(ws)
    steps = 8

    def body(*refs):
        own_half = (pl.program_id(0) // (steps // 2)) == lax.axis_index("c")
        for a in range(n):
            w = refs[a][...]
            g = jnp.where(own_half, refs[n + a][...], refs[2 * n + a][...])[:, :w.shape[1]]
            d, nm, nv = _adamw_math(w, g, refs[3 * n + a][...], refs[4 * n + a][...])
            refs[5 * n + a][...] = g
            refs[6 * n + a][...] = d
            refs[7 * n + a][...] = nm
            refs[8 * n + a][...] = nv

    specs = [pl.BlockSpec((w.shape[0] // steps, w.shape[1]), lambda i: (i, 0)) for w in ws]
    half_specs = [pl.BlockSpec((g.shape[0] // (steps // 2), g.shape[1]), lambda i: (i % (steps // 2), 0)) for g in g_mine]
    shapes = [jax.ShapeDtypeStruct(w.shape, F32) for w in ws]
    res = pl.pallas_call(
        body, name="adamw_big", grid=(steps,), in_specs=specs + half_specs * 2 + specs * 2, out_specs=specs * 4,
        out_shape=shapes * 4, compiler_params=_cparams(),
    )(*ws, *g_mine, *g_theirs, *ms, *vs)
    return res[:n], res[n:2 * n], res[2 * n:3 * n], res[3 * n:]


def _adamw_in(w, g_mine, g_theirs, m, v):
    half = D // 2

    def body(w_ref, gm_ref, gt_ref, m_ref, v_ref, g_out, d_out, nm_out, nv_out, g_ref):
        south = lax.axis_index("c") == 0
        g_ref[0:half, :] = jnp.where(south, gm_ref[...], gt_ref[...])
        g_ref[half:D, :] = jnp.where(south, gt_ref[...], gm_ref[...])
        g = g_ref[0:CW, :]
        d, nm, nv = _adamw_math(w_ref[...], g, m_ref[...], v_ref[...])
        g_out[...] = g
        d_out[...] = d
        nm_out[...] = nm
        nv_out[...] = nv

    spec = pl.BlockSpec((CW, LANES), lambda i: (0, i))
    half_spec = pl.BlockSpec((half, LANES), lambda i: (0, i))
    return pl.pallas_call(
        body, name="adamw_in", grid=(D // LANES,), in_specs=[spec, half_spec, half_spec, spec, spec], out_specs=[spec] * 4,
        out_shape=[jax.ShapeDtypeStruct((CW, D), F32)] * 4, scratch_shapes=[pltpu.VMEM((D, LANES), F32)],
        compiler_params=_cparams(),
    )(w, g_mine, g_theirs, m, v)


NORM_NAMES = ("pre_mix_norm", "post_mix_norm", "pre_mlp_norm", "post_mlp_norm")
SMALL_NAMES = NORM_NAMES + ("gdn_conv_w", "fox_f_bias", "gdn_dt_bias", "gdn_a_log", "fox_out_norm", "gdn_out_norm")
CONV_COLS = 3 * DGDN // NCHIP


def _small_allreduce(d_norms, d_conv, sums, d_fox_norm, d_gdn_norm):
    def body(*refs):
        dn_ref, dconv_ref, sums_ref, dfn_ref, dgn_ref = refs[:5]
        outs = refs[5:10]
        g_norms, g_conv, g_sums, g_fn, g_gn, send_sems, recv_sems, local_sem = refs[10:]
        x, y, c = _place()
        me = 4 * x + 2 * y + c
        g_norms[me] = dn_ref[...]
        g_sums[me] = sums_ref[...]
        g_fn[me] = dfn_ref[...]
        g_gn[me] = dgn_ref[...]

        def conv_cols(chip_index):
            return dconv_ref.at[:, pl.ds(pl.multiple_of(chip_index * CONV_COLS, LANES), CONV_COLS)]

        own = pltpu.make_async_copy(conv_cols(2 * x + y), g_conv.at[me], local_sem)
        own.start()
        copies = []
        for k in range(1, NDEV):
            px, py, pc = x ^ ((k >> 2) & 1), y ^ ((k >> 1) & 1), c ^ (k & 1)
            pairs = [(dn_ref, g_norms), (conv_cols(2 * px + py), g_conv), (sums_ref, g_sums), (dfn_ref, g_fn), (dgn_ref, g_gn)]
            for a, (src, dst) in enumerate(pairs):
                cp = pltpu.make_async_remote_copy(
                    src_ref=src, dst_ref=dst.at[me], send_sem=send_sems.at[5 * (k - 1) + a],
                    recv_sem=recv_sems.at[5 * (k - 1) + a], device_id=(px, py, pc), device_id_type=MESH)
                cp.start()
                copies.append(cp)
        own.wait()
        for cp in copies:
            cp.wait()

        def total(buf):
            acc = buf[0]
            for i in range(1, NDEV):
                acc = acc + buf[i]
            return acc

        for out, buf in zip(outs, (g_norms, g_conv, g_sums, g_fn, g_gn)):
            out[...] = total(buf)

    n_sem = 5 * (NDEV - 1)
    shapes = [(4, D), (CONV_K, CONV_COLS), (8, LANES), (1, LANES), (1, LANES)]
    return pl.pallas_call(
        body, name="small_allreduce", in_specs=[VMEM] * 5, out_specs=[VMEM] * 5,
        out_shape=[jax.ShapeDtypeStruct(s, F32) for s in shapes],
        scratch_shapes=[pltpu.VMEM((NDEV,) + s, F32) for s in shapes]
        + [pltpu.SemaphoreType.DMA((n_sem,)), pltpu.SemaphoreType.DMA((n_sem,)), pltpu.SemaphoreType.DMA],
        compiler_params=_cparams(),
    )(d_norms, d_conv, sums, d_fox_norm, d_gdn_norm)


def _small_adamw(totals, ws, ms, vs):
    n = len(SMALL_NAMES)

    def body(*refs):
        t_norms, t_conv, t_sums, t_fn, t_gn = [r[...] for r in refs[:5]]
        w_refs, m_refs, v_refs = refs[5:5 + n], refs[5 + n:5 + 2 * n], refs[5 + 2 * n:5 + 3 * n]
        outs = refs[5 + 3 * n:]
        grads = [t_norms[i:i + 1, :] for i in range(4)] + [
            t_conv, t_sums[0:1, 0:NFH], t_sums[1:2, 0:NGH], t_sums[2:3, 0:NGH], t_fn[:, 0:FHD], t_gn]
        for a in range(n):
            d, nm, nv = _adamw_math(w_refs[a][...], grads[a], m_refs[a][...], v_refs[a][...])
            outs[a][...] = grads[a]
            outs[n + a][...] = d
            outs[2 * n + a][...] = nm
            outs[3 * n + a][...] = nv

    def whole(arr):
        return pl.BlockSpec(arr.shape, lambda i: (0, 0))

    res = pl.pallas_call(
        body, name="small_adamw", grid=(1,), in_specs=[whole(t) for t in totals] + [whole(w) for w in ws] * 3,
        out_specs=[whole(w) for w in ws] * 4, out_shape=[jax.ShapeDtypeStruct(w.shape, F32) for w in ws] * 4,
        compiler_params=_cparams(),
    )(*totals, *ws, *ms, *vs)
    return res[:n], res[n:2 * n], res[2 * n:3 * n], res[3 * n:]


CW = DPROJ // NCHIP
PROJ_RUNS = tuple((part * DFOX + hp * LANES, part * DFOX + (hp + 1) * LANES, (3 * hp + part) * LANES)
                  for hp in range(NPAIR) for part in range(3)) + (
    (1536, 1544, BLK_SMALL * LANES), (1544, 3080, BLK_GDN * LANES), (3080, 3088, BLK_SMALL * LANES + 8),
    (3088, 3600, BLK_GZ * LANES))


def _proj_pieces():
    pieces = []
    for lo, hi, at in PROJ_RUNS:
        while lo < hi:
            j = lo // CW
            end = min(hi, (j + 1) * CW)
            pieces.append((j, lo - j * CW, at, end - lo))
            at, lo = at + end - lo, end
    return pieces


RT = 256


def _to_padded_rows(gathered):
    def body(src_ref, out_ref, blocks_ref, rows_ref):
        blocks_ref[...] = src_ref[...].astype(F32)
        rows_ref[...] = jnp.zeros_like(rows_ref)
        for j, start, at, n in _proj_pieces():
            rows_ref[at:at + n, :] = blocks_ref[j, start:start + n, :]
        out_ref[...] = rows_ref[...].astype(out_ref.dtype)

    return pl.pallas_call(
        body, name="proj_rows_in", grid=(D // RT,), in_specs=[pl.BlockSpec((NCHIP, D, RT), lambda i: (0, 0, i))],
        out_specs=pl.BlockSpec((DPROJ_PAD, RT), lambda i: (0, i)), out_shape=jax.ShapeDtypeStruct((DPROJ_PAD, D), gathered.dtype),
        scratch_shapes=[pltpu.VMEM((NCHIP, D, RT), F32), pltpu.VMEM((DPROJ_PAD, RT), F32)], compiler_params=_cparams(),
    )(gathered)


def _from_padded_rows(w):
    def body(src_ref, out_ref, rows_ref, blocks_ref):
        rows_ref[...] = src_ref[...].astype(F32)
        blocks_ref[...] = jnp.zeros_like(blocks_ref)
        for j, start, at, n in _proj_pieces():
            blocks_ref[j, start:start + n, :] = rows_ref[at:at + n, :]
        out_ref[...] = blocks_ref[...].astype(out_ref.dtype)

    return pl.pallas_call(
        body, name="proj_rows_out", grid=(D // RT,), in_specs=[pl.BlockSpec((DPROJ_PAD, RT), lambda i: (0, i))],
        out_specs=pl.BlockSpec((NCHIP, D, RT), lambda i: (0, 0, i)), out_shape=jax.ShapeDtypeStruct((NCHIP, D, D), w.dtype),
        scratch_shapes=[pltpu.VMEM((DPROJ_PAD, RT), F32), pltpu.VMEM((NCHIP, D, RT), F32)], compiler_params=_cparams(),
    )(w)


def _local_step(x, target, win_p, late_weights, reduce_late, reduce_in, pre_mix_norm, fox_f_bias, fox_out_norm, conv_w,
                gdn_a_log, gdn_dt_bias, gdn_out_norm, post_mix_norm, pre_mlp_norm, post_mlp_norm):
    bias_vec = jnp.zeros((1, LANES), F32).at[0, 0:NFH].set(fox_f_bias).at[0, LANE_G:LANE_G + NGH].set(gdn_dt_bias)
    alog_vec = jnp.zeros((1, LANES), F32).at[0, LANE_G:LANE_G + NGH].set(gdn_a_log)
    w2 = jnp.concatenate([fox_out_norm, fox_out_norm], axis=1)

    h = _pre_norm(x, pre_mix_norm)
    proj = _matmul(h, win_p, tb=True, tm=2048, tn=768, tk=1024, name="mm_proj")
    gates = _gates(proj, bias_vec, alog_vec)
    mix, fox_o, lse, late_a = _fox_fwd(proj, gates, w2, exchange=late_weights[0])
    qkv = _gdn_pre(proj, conv_w)
    (u, w, qd, kd, a_intra, gl, t_inv), late_b = _gdn_prep(qkv, gates, exchange=late_weights[1])
    wout, wup3, wdown = late_weights[2](late_a, late_b)
    mix, gdn_raw, states = _gdn_scan(u, w, qd, kd, a_intra, gl, proj, gdn_out_norm, mix)
    mixed = _matmul(mix, wout, tm=2048, tk=1024, name="mm_out")
    x1, h2 = _post_mix(x, mixed, post_mix_norm, pre_mlp_norm)

    def relu2(acc):
        r = jnp.maximum(acc, 0.0)
        return acc, r * r

    up, act = _matmul(h2, wup3, b3=True, tm=1024, tn=1024, tk=1024, out_dtypes=(F32, BF16), epilogue=relu2, name="mm_up")
    y = _matmul(act, wdown, tm=2048, tk=1024, name="mm_down")
    dx2, dy, d_post_mlp, loss_row = _loss_head(x1, y, post_mlp_norm, target)

    dwdown = _matmul(act, dy, ta=True, tm=1024, tn=1024, tk=2048, out_dtypes=(BF16,), name="mm_dwdown")

    def relu2_bwd(acc, upv):
        return (acc * 2.0 * jnp.maximum(upv, 0.0),)

    dup = _matmul(dy, wdown, tb=True, tm=1024, tn=1024, tk=1024, out_dtypes=(BF16,), extra=(up,), epilogue=relu2_bwd,
                  name="mm_dact")
    dwup3 = _matmul(h2, dup, ta=True, tm=1024, tn=1024, tk=2048, out_dtypes=(BF16,), o3=True, name="mm_dwup")
    dh2 = _matmul(dup, wup3, tb=True, b3=True, tm=2048, tk=1024, name="mm_dh2")
    dx1, dmixed, d_pre_mlp, d_post_mix = _mid_bwd(dh2, x1, pre_mlp_norm, dx2, mixed, post_mix_norm)
    dwout = _matmul(mix, dmixed, ta=True, tm=1024, tn=1024, tk=2048, out_dtypes=(BF16,), name="mm_dwout")
    dmix = _matmul(dmixed, wout, tb=True, tm=2048, tk=1024, name="mm_dmix")

    dfox, delta, d_fox_norm, from_sibling = _fox_norm_bwd(fox_o, dmix, w2, exchange=reduce_late[0](dwout, dwup3, dwdown))
    dproj, dcum_fox, reduced_late = _fox_bwd(proj, dfox, gates, lse, delta, exchange=reduce_late[1](from_sibling))
    dproj, du, dw, dqd, dkd, da, dgl, d_gdn_norm = _gdn_scan_bwd(dmix, gdn_raw, proj, gdn_out_norm, u, w, qd, kd,
                                                                 a_intra, gl, states, dproj)
    dqkv, dgates_gdn = _gdn_prep_bwd(qkv, gates, t_inv, du, dw, dqd, dkd, da, dgl)
    dproj, d_conv = _gdn_pre_bwd(proj, conv_w, dqkv, dproj)
    dproj, sums = _gates_bwd(proj, bias_vec, alog_vec, dgates_gdn, dcum_fox, dproj)

    dwin_p = _matmul(dproj, h, ta=True, tm=1280, tn=1024, tk=2048, out_dtypes=(BF16,), name="mm_dwin")
    exchange_in = reduce_in(dwin_p)
    dh = _matmul(dproj, win_p, tm=2048, tk=1280, name="mm_dh", exchange=exchange_in)
    dh, reduced_in = dh if exchange_in is not None else (dh, [])
    grad_x, d_pre_mix = _pre_norm_bwd(dh, x, pre_mix_norm, dx1)

    d_norms = jnp.concatenate([d_pre_mix, d_post_mix, d_pre_mlp, d_post_mlp], axis=0)
    return loss_row[0, 0], grad_x, (d_norms, d_conv, sums, d_fox_norm, d_gdn_norm), reduced_late, reduced_in


def kernel(x, pre_mix_norm, w_in, fox_f_bias, fox_out_norm, gdn_conv_w, gdn_a_log, gdn_dt_bias, gdn_out_norm, w_out, post_mix_norm, pre_mlp_norm, w_up, w_down, post_mlp_norm, loss_target, m_pre_mix_norm, m_w_in, m_fox_f_bias, m_fox_out_norm, m_gdn_conv_w, m_gdn_a_log, m_gdn_dt_bias, m_gdn_out_norm, m_w_out, m_post_mix_norm, m_pre_mlp_norm, m_w_up, m_w_down, m_post_mlp_norm, v_pre_mix_norm, v_w_in, v_fox_f_bias, v_fox_out_norm, v_gdn_conv_w, v_gdn_a_log, v_gdn_dt_bias, v_gdn_out_norm, v_w_out, v_post_mix_norm, v_pre_mlp_norm, v_w_up, v_w_down, v_post_mlp_norm):
    weights = dict(pre_mix_norm=pre_mix_norm, w_in=w_in, fox_f_bias=fox_f_bias, fox_out_norm=fox_out_norm, gdn_conv_w=gdn_conv_w,
                   gdn_a_log=gdn_a_log, gdn_dt_bias=gdn_dt_bias, gdn_out_norm=gdn_out_norm, w_out=w_out, post_mix_norm=post_mix_norm,
                   pre_mlp_norm=pre_mlp_norm, w_up=w_up, w_down=w_down, post_mlp_norm=post_mlp_norm)
    m_in = dict(pre_mix_norm=m_pre_mix_norm, w_in=m_w_in, fox_f_bias=m_fox_f_bias, fox_out_norm=m_fox_out_norm, gdn_conv_w=m_gdn_conv_w,
                gdn_a_log=m_gdn_a_log, gdn_dt_bias=m_gdn_dt_bias, gdn_out_norm=m_gdn_out_norm, w_out=m_w_out, post_mix_norm=m_post_mix_norm,
                pre_mlp_norm=m_pre_mlp_norm, w_up=m_w_up, w_down=m_w_down, post_mlp_norm=m_post_mlp_norm)
    v_in = dict(pre_mix_norm=v_pre_mix_norm, w_in=v_w_in, fox_f_bias=v_fox_f_bias, fox_out_norm=v_fox_out_norm, gdn_conv_w=v_gdn_conv_w,
                gdn_a_log=v_gdn_a_log, gdn_dt_bias=v_gdn_dt_bias, gdn_out_norm=v_gdn_out_norm, w_out=v_w_out, post_mix_norm=v_post_mix_norm,
                pre_mlp_norm=v_pre_mlp_norm, w_up=v_w_up, w_down=v_w_down, post_mlp_norm=v_post_mlp_norm)
    order_w = ("pre_mix_norm", "w_in", "fox_f_bias", "fox_out_norm", "gdn_conv_w", "gdn_a_log", "gdn_dt_bias", "gdn_out_norm", "w_out",
               "post_mix_norm", "pre_mlp_norm", "w_up", "w_down", "post_mlp_norm")
    big = ("w_in", "w_out", "w_up", "w_down")

    def row(v):
        return v if v.ndim == 2 else v.reshape(1, -1)

    win_shard = jnp.pad(w_in.T.astype(BF16), ((0, D - CW), (0, 0)))
    win_g, conv_g = _run_exchange(_allgather_exchange([win_shard], whole=[gdn_conv_w]), "weights_allgather_in")
    win_p = _to_padded_rows(_with_own(win_g, win_shard))
    conv_full = _with_own(conv_g, gdn_conv_w).transpose(1, 0, 2).reshape(CONV_K, 3 * DGDN)
    late_shards = [weights[n].astype(BF16) for n in big[1:]]

    def resolve_late(gathered_a, gathered_b):
        wout_g, wup3, wdown_g = [_with_own(g, own) for g, own in zip(list(gathered_a) + list(gathered_b), late_shards)]
        return wout_g.reshape(D, D), wup3, wdown_g.reshape(DFF, D)

    pair_sums, late_blocks = {}, []

    def chip_exchange_of(names, blocks, theirs):
        for n, s in zip(names, _pair_sum(blocks, theirs, "grads_pair_sum_" + names[0])):
            pair_sums[n] = s
        return _chip_exchange([pair_sums[n] for n in names])

    def late_pair_exchange(dwout, dwup3, dwdown):
        late_blocks.extend([dwout.reshape(NCHIP, D // NCHIP, D), dwup3, dwdown.reshape(NCHIP, DFF // NCHIP, D)])
        return _pair_exchange(late_blocks)

    def late_chip_exchange(theirs):
        return chip_exchange_of(big[1:], late_blocks, theirs)

    def reduce_in(dwin_p):
        blocks = [_from_padded_rows(dwin_p)]
        return chip_exchange_of(big[:1], blocks, _run_exchange(_pair_exchange(blocks), "grads_pair_exchange_w_in"))

    loss_local, grad_x, small, received_late, received_in = _local_step(
        x[0], loss_target[0], win_p, (_allgather_exchange(late_shards[:2]), _allgather_exchange(late_shards[2:]), resolve_late),
        (late_pair_exchange, late_chip_exchange), reduce_in, row(pre_mix_norm), fox_f_bias, row(fox_out_norm), conv_full,
        gdn_a_log, gdn_dt_bias,
        row(gdn_out_norm), row(post_mix_norm), row(pre_mlp_norm), row(post_mlp_norm))
    loss = lax.psum(loss_local, ("x", "y", "c"))

    g_mine = _chip_sum([pair_sums[n] for n in big], list(received_in) + list(received_late))
    g_theirs = _run_exchange(_pair_share(g_mine), "grads_pair_share")

    g_small, d_small, nm_small, nv_small = _small_adamw(
        _small_allreduce(*small), [row(weights[n]) for n in SMALL_NAMES], [row(m_in[n]) for n in SMALL_NAMES],
        [row(v_in[n]) for n in SMALL_NAMES])

    g_big, d_big, nm_big, nv_big = _adamw_big([weights[n] for n in big[1:]], g_mine[1:], g_theirs[1:],
                                              [m_in[n] for n in big[1:]], [v_in[n] for n in big[1:]])
    in_t = _adamw_in(w_in.T, g_mine[0], g_theirs[0], m_w_in.T, v_w_in.T)

    grads, delta, new_m, new_v = {}, {}, {}, {}
    grads["w_in"], delta["w_in"], new_m["w_in"], new_v["w_in"] = [t.T for t in in_t]
    for i, n in enumerate(big[1:]):
        grads[n], delta[n], new_m[n], new_v[n] = g_big[i], d_big[i], nm_big[i], nv_big[i]
    for i, n in enumerate(SMALL_NAMES):
        shape = weights[n].shape
        grads[n], delta[n], new_m[n], new_v[n] = (g_small[i].reshape(shape), d_small[i].reshape(shape),
                                                  nm_small[i].reshape(shape), nv_small[i].reshape(shape))
    return (loss, grad_x[None], *[grads[n] for n in order_w], *[delta[n] for n in order_w], *[new_m[n] for n in order_w],
            *[new_v[n] for n in order_w])
```

```python
import jax
import jax.numpy as jnp
from jax import lax
from jax.experimental import pallas as pl
from jax.experimental.pallas import tpu as pltpu

F32 = jnp.float32
BF16 = jnp.bfloat16
MESH = pl.DeviceIdType.MESH

S = 2048
D = 1024
NFH, FHD = 8, 64
NPAIR = NFH // 2
NGH, GHD = 4, 128
DFOX = NFH * FHD
DGDN = NGH * GHD
CHUNK = 64
NCH = S // CHUNK
CONV_K = 4
DFF = 4 * D
EPS = 1e-6
DPROJ = 3600
LANES = 128
DPROJ_PAD = 3840
BLK_GDN = 12
BLK_GZ = 24
BLK_SMALL = 28
NCHIP = 4
NDEV = 8
VMEM_LIMIT = 56 * 1024 * 1024

ADAM_LR = 0.001
ADAM_B1 = 0.9
ADAM_B2 = 0.999
ADAM_EPS = 1e-08
ADAM_WD = 0.01
ADAM_STEP = 10


def _cparams(**kw):
    return pltpu.CompilerParams(vmem_limit_bytes=VMEM_LIMIT, **kw)


def _dn(ca, cb):
    return (((ca,), (cb,)), ((), ()))


def _dot(a, b, ca=1, cb=0):
    return lax.dot_general(a.astype(BF16), b.astype(BF16), _dn(ca, cb), preferred_element_type=F32)


def _hdot(a, b, ca=1, cb=0):
    return lax.dot_general(a.astype(F32), b.astype(F32), _dn(ca, cb), precision=lax.Precision.HIGHEST,
                           preferred_element_type=F32)


def _dot3(a, b, ca=1, cb=0):
    a_hi, b_hi = a.astype(BF16), b.astype(BF16)
    a_lo, b_lo = (a - a_hi.astype(F32)).astype(BF16), (b - b_hi.astype(F32)).astype(BF16)
    dn = _dn(ca, cb)
    return (lax.dot_general(a_hi, b_hi, dn, preferred_element_type=F32)
            + (lax.dot_general(a_hi, b_lo, dn, preferred_element_type=F32)
               + lax.dot_general(a_lo, b_hi, dn, preferred_element_type=F32)))


@jax.custom_vjp
def _mm_nn(a, b):
    return _dot(a, b, 1, 0)


def _mm_nn_fwd(a, b):
    return _dot(a, b, 1, 0), (a, b)


def _mm_nn_bwd(res, g):
    a, b = res
    return _dot(g, b, 1, 1), _dot(a, g, 0, 0)


_mm_nn.defvjp(_mm_nn_fwd, _mm_nn_bwd)


@jax.custom_vjp
def _mm_nt(a, b):
    return _dot(a, b, 1, 1)


def _mm_nt_fwd(a, b):
    return _dot(a, b, 1, 1), (a, b)


def _mm_nt_bwd(res, g):
    a, b = res
    return _dot(g, b, 1, 0), _dot(g, a, 0, 0)


_mm_nt.defvjp(_mm_nt_fwd, _mm_nt_bwd)


@jax.custom_vjp
def _saved_inverse(m, t_inv):
    del m
    return t_inv


def _saved_inverse_fwd(m, t_inv):
    del m
    return t_inv, t_inv


def _saved_inverse_bwd(t_inv, g):
    return -_dot3(_dot3(t_inv, g, 0, 0), t_inv, 1, 1), jnp.zeros_like(t_inv)


_saved_inverse.defvjp(_saved_inverse_fwd, _saved_inverse_bwd)


def _sigmoid(z):
    return 1.0 / (1.0 + jnp.exp(-z))


def _softplus(z):
    return jnp.maximum(z, 0.0) + jnp.log(1.0 + jnp.exp(-jnp.abs(z)))


def _silu(z):
    return z * _sigmoid(z)


def _rms_scale(x):
    return lax.rsqrt(jnp.mean(x * x, axis=-1, keepdims=True) + EPS)


def _rms_bwd(x, w, g):
    r = _rms_scale(x)
    gw = g * w
    dx = r * gw - x * (r * r * r) * jnp.mean(gw * x, axis=-1, keepdims=True)
    return dx, g * x * r


def _matmul(a, b, *, name, ta=False, tb=False, tm=512, tn=512, tk=512, out_dtypes=(F32,), b3=False, o3=False,
            extra=(), epilogue=None, exchange=None):
    m, k = (a.shape[1], a.shape[0]) if ta else a.shape
    if b3:
        n = b.shape[1] if tb else b.shape[0] * b.shape[2]
        kb = b.shape[0] * b.shape[2] if tb else b.shape[1]
    else:
        n, kb = (b.shape[0], b.shape[1]) if tb else (b.shape[1], b.shape[0])
    assert kb == k, (name, kb, k)
    tm, tn, tk = min(tm, m), min(tn, n), min(tk, k)
    assert m % tm == 0 and n % tn == 0 and k % tk == 0, (name, m, n, k, tm, tn, tk)
    nk = k // tk
    n_extra = len(extra)
    n_out = len(out_dtypes)
    grid = (m // tm, n // tn, nk)
    ex_in, ex_in_specs, ex_out_specs, ex_out_shape, ex_scratch = _hosted(exchange)

    def body(*refs):
        a_ref, b_ref = refs[0], refs[1]
        extra_refs = refs[2:2 + n_extra]
        first_out = 2 + n_extra + len(ex_in)
        out_refs = refs[first_out:first_out + n_out]
        ex_refs = refs[2 + n_extra:first_out] + refs[first_out + n_out:first_out + n_out + len(ex_out_shape)] + refs[-2:]
        step = [pl.program_id(d) for d in range(3)]

        if exchange is not None:
            @pl.when((step[0] == 0) & (step[1] == 0) & (step[2] == 0))
            def _():
                exchange.start(*exchange.split(ex_refs))

        def finish(acc):
            outs = (acc,) if epilogue is None else epilogue(acc, *[r[...] for r in extra_refs])
            for o_ref, val in zip(out_refs, outs):
                o_ref[...] = val.astype(o_ref.dtype)

        part = _dot(a_ref[...], b_ref[...], 0 if ta else 1, 1 if tb else 0)
        if nk == 1:
            finish(part)
        else:
            acc_ref = refs[first_out + n_out + len(ex_out_shape)]

            @pl.when(step[2] == 0)
            def _():
                acc_ref[...] = part

            @pl.when(step[2] > 0)
            def _():
                acc_ref[...] += part

            @pl.when(step[2] == nk - 1)
            def _():
                finish(acc_ref[...])

        if exchange is not None:
            @pl.when((step[0] == grid[0] - 1) & (step[1] == grid[1] - 1) & (step[2] == nk - 1))
            def _():
                exchange.finish(*exchange.split(ex_refs))

    a_spec = pl.BlockSpec((tk, tm), lambda i, j, kk: (kk, i)) if ta else pl.BlockSpec((tm, tk), lambda i, j, kk: (i, kk))
    if b3 and tb:
        assert b.shape[2] == tk
        b_spec = pl.BlockSpec((None, tn, tk), lambda i, j, kk: (kk, j, 0))
    elif b3:
        assert b.shape[2] == tn
        b_spec = pl.BlockSpec((None, tk, tn), lambda i, j, kk: (j, kk, 0))
    elif tb:
        b_spec = pl.BlockSpec((tn, tk), lambda i, j, kk: (j, kk))
    else:
        b_spec = pl.BlockSpec((tk, tn), lambda i, j, kk: (kk, j))
    tile = pl.BlockSpec((tm, tn), lambda i, j, kk: (i, j))
    out_specs = [tile] * n_out
    out_shape = [jax.ShapeDtypeStruct((m, n), dt) for dt in out_dtypes]
    if o3:
        out_specs[0] = pl.BlockSpec((None, tm, tn), lambda i, j, kk: (j, i, 0))
        out_shape[0] = jax.ShapeDtypeStruct((n // tn, m, tn), out_dtypes[0])
    res = pl.pallas_call(
        body, name=name, grid=grid,
        in_specs=[a_spec, b_spec] + [tile] * n_extra + ex_in_specs, out_specs=out_specs + ex_out_specs,
        out_shape=out_shape + ex_out_shape,
        scratch_shapes=([pltpu.VMEM((tm, tn), F32)] if nk > 1 else []) + ex_scratch,
        compiler_params=_cparams(),
    )(a, b, *extra, *ex_in)
    if exchange is not None:
        return (res[0] if n_out == 1 else res[:n_out]), res[n_out:]
    return res[0] if n_out == 1 else res


TR = 256


def _row_spec(cols):
    return pl.BlockSpec((TR, cols), lambda i: (i, 0))


def _vec_spec(cols):
    return pl.BlockSpec((1, cols), lambda i: (0, 0))


def _pre_norm(x, w, exchange=None):
    ex_in, ex_in_specs, ex_out_specs, ex_out_shape, ex_scratch = _hosted(exchange)

    def body(*refs):
        x_ref, w_ref, h_ref = refs[0], refs[1], refs[2 + len(ex_in)]
        ex_refs = refs[2:2 + len(ex_in)] + refs[3 + len(ex_in):]
        if exchange is not None:
            @pl.when(pl.program_id(0) == 0)
            def _():
                exchange.start(*exchange.split(ex_refs))

        xv = x_ref[...]
        h_ref[...] = (xv * _rms_scale(xv) * w_ref[...]).astype(BF16)

        if exchange is not None:
            @pl.when(pl.program_id(0) == S // TR - 1)
            def _():
                exchange.finish(*exchange.split(ex_refs))

    res = pl.pallas_call(
        body, name="pre_norm", grid=(S // TR,), in_specs=[_row_spec(D), _vec_spec(D)] + ex_in_specs,
        out_specs=[_row_spec(D)] + ex_out_specs, out_shape=[jax.ShapeDtypeStruct((S, D), BF16)] + ex_out_shape,
        scratch_shapes=ex_scratch, compiler_params=_cparams(),
    )(x, w, *ex_in)
    return res[0], res[1:]


def _post_mix(x, mixed, w_post, w_pre_mlp):
    def body(x_ref, m_ref, wp_ref, wm_ref, x1_ref, h2_ref):
        mv = m_ref[...]
        x1 = x_ref[...] + mv * _rms_scale(mv) * wp_ref[...]
        x1_ref[...] = x1
        h2_ref[...] = (x1 * _rms_scale(x1) * wm_ref[...]).astype(BF16)

    return pl.pallas_call(
        body, name="post_mix", grid=(S // TR,),
        in_specs=[_row_spec(D), _row_spec(D), _vec_spec(D), _vec_spec(D)], out_specs=[_row_spec(D), _row_spec(D)],
        out_shape=[jax.ShapeDtypeStruct((S, D), F32), jax.ShapeDtypeStruct((S, D), BF16)], compiler_params=_cparams(),
    )(x, mixed, w_post, w_pre_mlp)


def _loss_head(x1, y, w_post_mlp, target):
    def body(x1_ref, y_ref, w_ref, t_ref, dx2_ref, dy_ref, dw_ref, loss_ref):
        i = pl.program_id(0)
        yv = y_ref[...]
        w = w_ref[...]
        x2 = x1_ref[...] + yv * _rms_scale(yv) * w
        err = x2 - t_ref[...]
        dx2 = err * (1.0 / D)
        dx2_ref[...] = dx2
        dy, dwt = _rms_bwd(yv, w, dx2)
        dy_ref[...] = dy.astype(BF16)

        @pl.when(i == 0)
        def _():
            dw_ref[...] = jnp.zeros_like(dw_ref)
            loss_ref[...] = jnp.zeros_like(loss_ref)

        dw_ref[...] += jnp.sum(dwt, axis=0, keepdims=True)
        part = 0.5 * jnp.sum(jnp.mean(err * err, axis=-1, keepdims=True), axis=0, keepdims=True)
        loss_ref[...] += jnp.broadcast_to(part, loss_ref.shape)

    return pl.pallas_call(
        body, name="loss_head", grid=(S // TR,),
        in_specs=[_row_spec(D), _row_spec(D), _vec_spec(D), _row_spec(D)],
        out_specs=[_row_spec(D), _row_spec(D), _vec_spec(D), _vec_spec(LANES)],
        out_shape=[jax.ShapeDtypeStruct((S, D), F32), jax.ShapeDtypeStruct((S, D), BF16),
                   jax.ShapeDtypeStruct((1, D), F32), jax.ShapeDtypeStruct((1, LANES), F32)],
        compiler_params=_cparams(),
    )(x1, y, w_post_mlp, target)


def _mid_bwd(dh2, x1, w_pre_mlp, dx2, mixed, w_post):
    def body(dh2_ref, x1_ref, wm_ref, dx2_ref, m_ref, wp_ref, dx1_ref, dm_ref, dwm_ref, dwp_ref):
        i = pl.program_id(0)
        dxa, dwm = _rms_bwd(x1_ref[...], wm_ref[...], dh2_ref[...])
        dx1 = dx2_ref[...] + dxa
        dx1_ref[...] = dx1
        dm, dwp = _rms_bwd(m_ref[...], wp_ref[...], dx1)
        dm_ref[...] = dm.astype(BF16)

        @pl.when(i == 0)
        def _():
            dwm_ref[...] = jnp.zeros_like(dwm_ref)
            dwp_ref[...] = jnp.zeros_like(dwp_ref)

        dwm_ref[...] += jnp.sum(dwm, axis=0, keepdims=True)
        dwp_ref[...] += jnp.sum(dwp, axis=0, keepdims=True)

    return pl.pallas_call(
        body, name="mid_bwd", grid=(S // TR,),
        in_specs=[_row_spec(D), _row_spec(D), _vec_spec(D), _row_spec(D), _row_spec(D), _vec_spec(D)],
        out_specs=[_row_spec(D), _row_spec(D), _vec_spec(D), _vec_spec(D)],
        out_shape=[jax.ShapeDtypeStruct((S, D), F32), jax.ShapeDtypeStruct((S, D), BF16),
                   jax.ShapeDtypeStruct((1, D), F32), jax.ShapeDtypeStruct((1, D), F32)],
        compiler_params=_cparams(),
    )(dh2, x1, w_pre_mlp, dx2, mixed, w_post)


def _pre_norm_bwd(dh, x, w, dx1):
    def body(dh_ref, x_ref, w_ref, dx1_ref, dx_ref, dw_ref):
        i = pl.program_id(0)
        dxa, dwt = _rms_bwd(x_ref[...], w_ref[...], dh_ref[...])
        dx_ref[...] = dx1_ref[...] + dxa

        @pl.when(i == 0)
        def _():
            dw_ref[...] = jnp.zeros_like(dw_ref)

        dw_ref[...] += jnp.sum(dwt, axis=0, keepdims=True)

    return pl.pallas_call(
        body, name="pre_norm_bwd", grid=(S // TR,),
        in_specs=[_row_spec(D), _row_spec(D), _vec_spec(D), _row_spec(D)], out_specs=[_row_spec(D), _vec_spec(D)],
        out_shape=[jax.ShapeDtypeStruct((S, D), F32), jax.ShapeDtypeStruct((1, D), F32)], compiler_params=_cparams(),
    )(dh, x, w, dx1)


BQ = 256
NQ = S // BQ
LANE_BETA, LANE_G = 8, 12


def _gate_lanes(shape):
    lane = lax.broadcasted_iota(jnp.int32, shape, 1)
    return lane < LANE_BETA, (lane >= LANE_BETA) & (lane < LANE_G), (lane >= LANE_G) & (lane < LANE_G + NGH)


def _gates(proj, bias_vec, alog_vec):
    def body(s_ref, b_ref, a_ref, o_ref, carry_ref):
        i = pl.program_id(0)

        @pl.when(i == 0)
        def _():
            carry_ref[...] = jnp.zeros_like(carry_ref)

        z = s_ref[...] + b_ref[...]
        tail = jnp.log(1.0 + jnp.exp(-jnp.abs(z)))
        sp = jnp.maximum(z, 0.0) + tail
        lf = jnp.minimum(z, 0.0) - tail
        r = lax.broadcasted_iota(jnp.int32, (BQ, BQ), 0)
        c = lax.broadcasted_iota(jnp.int32, (BQ, BQ), 1)
        tri = (c <= r).astype(F32)
        cum = _hdot(tri, lf) + carry_ref[...]
        carry_ref[...] = cum[BQ - 1:BQ, :]
        is_fox, is_beta, is_g = _gate_lanes(z.shape)
        o_ref[...] = jnp.where(is_fox, cum, jnp.where(is_beta, _sigmoid(z), jnp.where(is_g, -jnp.exp(a_ref[...]) * sp, 0.0)))

    return pl.pallas_call(
        body, name="gates", grid=(NQ,),
        in_specs=[pl.BlockSpec((BQ, LANES), lambda i: (i, BLK_SMALL)), _vec_spec(LANES), _vec_spec(LANES)],
        out_specs=pl.BlockSpec((BQ, LANES), lambda i: (i, 0)), out_shape=jax.ShapeDtypeStruct((S, LANES), F32),
        scratch_shapes=[pltpu.VMEM((1, LANES), F32)], compiler_params=_cparams(),
    )(proj, bias_vec, alog_vec)


def _gates_bwd(proj, bias_vec, alog_vec, dgates_gdn, dcum_fox, dproj):
    def body(s_ref, b_ref, a_ref, dg_ref, dc_ref, dproj_in, dproj_ref, red_ref, carry_ref):
        del dproj_in
        i = pl.program_id(0)

        @pl.when(i == 0)
        def _():
            carry_ref[...] = jnp.zeros_like(carry_ref)
            red_ref[...] = jnp.zeros_like(red_ref)

        z = s_ref[...] + b_ref[...]
        dg = dg_ref[...] + dc_ref[...]
        r = lax.broadcasted_iota(jnp.int32, (BQ, BQ), 0)
        c = lax.broadcasted_iota(jnp.int32, (BQ, BQ), 1)
        upper = (c >= r).astype(F32)
        dlf = _hdot(upper, dg) + carry_ref[...]
        carry_ref[...] = dlf[0:1, :]
        sig = _sigmoid(z)
        g_scale = -jnp.exp(a_ref[...])
        is_fox, is_beta, is_g = _gate_lanes(z.shape)
        ds = jnp.where(is_fox, dlf * (1.0 - sig), jnp.where(is_beta, dg * sig * (1.0 - sig), jnp.where(is_g, dg * g_scale * sig, 0.0)))
        dproj_ref[:, 0:LANES] = ds.astype(BF16)
        dproj_ref[:, LANES:2 * LANES] = jnp.zeros((BQ, LANES), BF16)
        dalog = jnp.where(is_g, dg * g_scale * _softplus(z), 0.0)
        sums = jnp.sum(ds, axis=0, keepdims=True)
        red_ref[0:1, :] += jnp.where(is_fox[0:1], sums, 0.0)
        red_ref[1:2, :] += pltpu.roll(jnp.where(is_g[0:1], sums, 0.0), LANES - LANE_G, 1)
        red_ref[2:3, :] += pltpu.roll(jnp.sum(dalog, axis=0, keepdims=True), LANES - LANE_G, 1)

    blk = pl.BlockSpec((BQ, LANES), lambda i: (NQ - 1 - i, 0))
    return pl.pallas_call(
        body, name="gates_bwd", grid=(NQ,),
        in_specs=[pl.BlockSpec((BQ, LANES), lambda i: (NQ - 1 - i, BLK_SMALL)), _vec_spec(LANES), _vec_spec(LANES), blk, blk,
                  pl.BlockSpec(memory_space=pl.ANY)],
        out_specs=[pl.BlockSpec((BQ, 2 * LANES), lambda i: (NQ - 1 - i, BLK_SMALL // 2)), pl.BlockSpec((8, LANES), lambda i: (0, 0))],
        out_shape=[jax.ShapeDtypeStruct((S, DPROJ_PAD), BF16), jax.ShapeDtypeStruct((8, LANES), F32)],
        input_output_aliases={5: 0},
        scratch_shapes=[pltpu.VMEM((1, LANES), F32)], compiler_params=_cparams(),
    )(proj, bias_vec, alog_vec, dgates_gdn, dcum_fox, dproj)


FOX_SCALE = FHD ** -0.5
FOX_PAIRS = 2
FOX_PAIRS_BWD = 2


def _head_mask(e):
    lane = lax.broadcasted_iota(jnp.int32, (1, LANES), 1)
    return (lane >= e * FHD) & (lane < (e + 1) * FHD)


def _lane_col(vals, index):
    lane = lax.broadcasted_iota(jnp.int32, vals.shape, 1)
    return jnp.sum(jnp.where(lane == index, vals, 0.0), axis=1, keepdims=True)


def _sublane_row(vals, index):
    row = lax.broadcasted_iota(jnp.int32, vals.shape, 0)
    return jnp.sum(jnp.where(row == index, vals, 0.0), axis=0, keepdims=True)


def _pair_cols(c0, c1):
    lane = lax.broadcasted_iota(jnp.int32, (c0.shape[0], 2), 1)
    return jnp.where(lane == 0, c0, c1)


def _split3(x):
    hi = x.astype(BF16).astype(F32)
    rest = x - hi
    mid = rest.astype(BF16).astype(F32)
    return hi, mid, (rest - mid).astype(BF16).astype(F32)


def _fox_operand(vals, e, cum, is_query):
    lane = lax.broadcasted_iota(jnp.int32, (1, LANES), 1)
    base = (1 - e) * FHD
    parts = _split3(cum)
    own = jnp.where(_head_mask(e), vals * FOX_SCALE if is_query else vals, 0.0)
    cum_at, ones_at = (base, base + 3) if is_query else (base + 3, base)
    sign = 1.0 if is_query else -1.0
    out = own + jnp.where((lane >= ones_at) & (lane < ones_at + 3), 1.0, 0.0)
    for i, part in enumerate(parts):
        out = out + jnp.where(lane == cum_at + i, sign * part, 0.0)
    return out.astype(BF16)


def _causal_block():
    return lax.broadcasted_iota(jnp.int32, (BQ, BQ), 1) <= lax.broadcasted_iota(jnp.int32, (BQ, BQ), 0)


def _head_rms(o, masks):
    o2 = o * o
    r = [lax.rsqrt(jnp.sum(jnp.where(mk, o2, 0.0), axis=1, keepdims=True) * (1.0 / FHD) + EPS) for mk in masks]
    return jnp.where(masks[0], r[0], r[1])


def _hosted(exchange):
    if exchange is None:
        return [], [], [], [], []
    return (exchange.inputs, [HBM] * len(exchange.inputs), [HBM] * len(exchange.out_shape), exchange.out_shape,
            exchange.sem_shapes())


def _fox_fwd(proj, gates, w2, exchange=None):
    ex_in, ex_in_specs, ex_out_specs, ex_out_shape, ex_scratch = _hosted(exchange)

    n_in = 3 * FOX_PAIRS + 2
    heads = [(pp, e) for pp in range(FOX_PAIRS) for e in range(2)]

    def body(*refs):
        qkv_refs, g_ref, w_ref = refs[:3 * FOX_PAIRS], refs[3 * FOX_PAIRS], refs[3 * FOX_PAIRS + 1]
        mix_ref, o_ref, lse_ref = refs[n_in + len(ex_in):n_in + 3 + len(ex_in)]
        ka_ref, vb_ref = refs[n_in + 3 + len(ex_in) + len(ex_out_shape):n_in + 5 + len(ex_in) + len(ex_out_shape)]
        ex_refs = refs[n_in:n_in + len(ex_in)] + refs[n_in + 3 + len(ex_in):n_in + 3 + len(ex_in) + len(ex_out_shape)] + refs[-2:]
        grp, qi = pl.program_id(0), pl.program_id(1)

        def head_index(pp, e):
            return 2 * (FOX_PAIRS * grp + pp) + e

        if exchange is not None:
            @pl.when((grp == 0) & (qi == 0))
            def _():
                exchange.start(*exchange.split(ex_refs))

        @pl.when(qi == 0)
        def _():
            gt = g_ref[...]
            for pp in range(FOX_PAIRS):
                kv = qkv_refs[3 * pp + 1][...]
                for e in range(2):
                    ka_ref[2 * pp + e] = _fox_operand(kv, e, _lane_col(gt, head_index(pp, e)), False)
                vb_ref[pp] = qkv_refs[3 * pp + 2][...].astype(BF16)

        masks = [_head_mask(0), _head_mask(1)]
        gt = g_ref[pl.ds(pl.multiple_of(qi * BQ, BQ), BQ), :]
        qs = [_fox_operand(qkv_refs[3 * pp][...], e, _lane_col(gt, head_index(pp, e)), True) for pp, e in heads]
        n = range(len(heads))

        def block(kj, carry, diagonal):
            rows = pl.ds(pl.multiple_of(kj * BQ, BQ), BQ)
            s = [_dot(qs[i], ka_ref[i, rows, :], 1, 1) for i in n]
            if diagonal:
                s = [jnp.where(_causal_block(), s[i], -jnp.inf) for i in n]
            m_new = [jnp.maximum(carry[i][0], jnp.max(s[i], axis=-1, keepdims=True)) for i in n]
            p = [jnp.exp(s[i] - m_new[i]) for i in n]
            alpha = [jnp.exp(carry[i][0] - m_new[i]) for i in n]
            l_new = [alpha[i] * carry[i][1] + jnp.sum(p[i], axis=-1, keepdims=True) for i in n]
            pv = [_dot(p[i], vb_ref[heads[i][0], rows, :]) for i in n]
            return tuple((m_new[i], l_new[i], alpha[i] * carry[i][2] + pv[i]) for i in n)

        one = (jnp.full((BQ, 1), -jnp.inf, F32), jnp.zeros((BQ, 1), F32), jnp.zeros((BQ, LANES), F32))
        below = lax.fori_loop(0, qi, lambda kj, carry: block(kj, carry, False), (one,) * len(heads))
        done = block(qi, below, True)
        for pp in range(FOX_PAIRS):
            (m0, l0, a0), (m1, l1, a1) = done[2 * pp], done[2 * pp + 1]
            o = jnp.where(masks[0], a0 / l0, a1 / l1)
            cols = slice(pp * LANES, (pp + 1) * LANES)
            o_ref[:, cols] = o
            mix_ref[:, cols] = (o * _head_rms(o, masks) * w_ref[...]).astype(BF16)
            lse_ref[pp] = _pair_cols(m0 + jnp.log(l0), m1 + jnp.log(l1))

        if exchange is not None:
            @pl.when((grp == NPAIR // FOX_PAIRS - 1) & (qi == NQ - 1))
            def _():
                exchange.finish(*exchange.split(ex_refs))

    qkv_specs = []
    for pp in range(FOX_PAIRS):
        qkv_specs.append(pl.BlockSpec((BQ, LANES), lambda g, i, pp=pp: (i, 3 * (FOX_PAIRS * g + pp))))
        qkv_specs.append(pl.BlockSpec((S, LANES), lambda g, i, pp=pp: (0, 3 * (FOX_PAIRS * g + pp) + 1)))
        qkv_specs.append(pl.BlockSpec((S, LANES), lambda g, i, pp=pp: (0, 3 * (FOX_PAIRS * g + pp) + 2)))
    blk = pl.BlockSpec((BQ, FOX_PAIRS * LANES), lambda g, i: (i, g))
    res = pl.pallas_call(
        body, name="fox_fwd", grid=(NPAIR // FOX_PAIRS, NQ),
        in_specs=qkv_specs + [pl.BlockSpec((S, LANES), lambda g, i: (0, 0)), pl.BlockSpec((1, LANES), lambda g, i: (0, 0))]
        + ex_in_specs,
        out_specs=[blk, blk, pl.BlockSpec((FOX_PAIRS, BQ, 2), lambda g, i: (g, i, 0))] + ex_out_specs,
        out_shape=[jax.ShapeDtypeStruct((S, D), BF16), jax.ShapeDtypeStruct((S, DFOX), F32),
                   jax.ShapeDtypeStruct((NPAIR, S, 2), F32)] + ex_out_shape,
        scratch_shapes=[pltpu.VMEM((2 * FOX_PAIRS, S, LANES), BF16), pltpu.VMEM((FOX_PAIRS, S, LANES), BF16)] + ex_scratch,
        compiler_params=_cparams(),
    )(*([proj] * (3 * FOX_PAIRS)), gates, w2, *ex_in)
    return res[0], res[1], res[2], res[3:]


def _fox_norm_bwd(o, dmix, w2, exchange=None):
    ex_in, ex_in_specs, ex_out_specs, ex_out_shape, ex_scratch = _hosted(exchange)

    def body(*refs):
        o_ref, g_ref, w_ref = refs[:3]
        do_ref, dl_ref, dw_ref = refs[3 + len(ex_in):6 + len(ex_in)]
        ex_refs = refs[3:3 + len(ex_in)] + refs[6 + len(ex_in):]
        hp, qi = pl.program_id(0), pl.program_id(1)

        if exchange is not None:
            @pl.when((hp == 0) & (qi == 0))
            def _():
                exchange.start(*exchange.split(ex_refs))

        masks = [_head_mask(0), _head_mask(1)]
        ov = o_ref[...]
        g = g_ref[...]
        r = _head_rms(ov, masks)
        gw = g * w_ref[...]
        gwo = gw * ov
        mean = [jnp.sum(jnp.where(mk, gwo, 0.0), axis=1, keepdims=True) * (1.0 / FHD) for mk in masks]
        do = r * gw - ov * (r * r * r) * jnp.where(masks[0], mean[0], mean[1])
        do_ref[...] = do.astype(BF16)
        doo = do * ov
        dl_ref[...] = _pair_cols(*[jnp.sum(jnp.where(mk, doo, 0.0), axis=1, keepdims=True) for mk in masks])

        @pl.when((hp == 0) & (qi == 0))
        def _():
            dw_ref[...] = jnp.zeros_like(dw_ref)

        dw_ref[...] += jnp.sum(g * ov * r, axis=0, keepdims=True)

        @pl.when((hp == NPAIR - 1) & (qi == NQ - 1))
        def _():
            dw = dw_ref[...]
            dw_ref[...] = dw + pltpu.roll(dw, FHD, 1)
            if exchange is not None:
                exchange.finish(*exchange.split(ex_refs))

    blk = pl.BlockSpec((BQ, LANES), lambda hp, i: (i, hp))
    vec = pl.BlockSpec((1, LANES), lambda hp, i: (0, 0))
    res = pl.pallas_call(
        body, name="fox_norm_bwd", grid=(NPAIR, NQ), in_specs=[blk, blk, vec] + ex_in_specs,
        out_specs=[blk, pl.BlockSpec((None, BQ, 2), lambda hp, i: (hp, i, 0)), vec] + ex_out_specs,
        out_shape=[jax.ShapeDtypeStruct((S, DFOX), BF16), jax.ShapeDtypeStruct((NPAIR, S, 2), F32),
                   jax.ShapeDtypeStruct((1, LANES), F32)] + ex_out_shape,
        scratch_shapes=ex_scratch, compiler_params=_cparams(),
    )(o, dmix, w2, *ex_in)
    return res[0], res[1], res[2], res[3:]


def _fox_bwd(proj, do, gates, lse, delta, exchange=None):
    ex_in, ex_in_specs, ex_out_specs, ex_out_shape, ex_scratch = _hosted(exchange)

    pg = FOX_PAIRS_BWD
    n_in = 3 * pg + 4
    heads = [(pp, e) for pp in range(pg) for e in range(2)]

    def body(*refs):
        qkv_refs = refs[:3 * pg]
        do_ref, g_ref, lse_ref, dl_ref = refs[3 * pg:n_in]
        dproj_ref, dc_ref = refs[n_in + len(ex_in):n_in + 2 + len(ex_in)]
        qa_ref, dq_ref = refs[n_in + 2 + len(ex_in) + len(ex_out_shape):n_in + 4 + len(ex_in) + len(ex_out_shape)]
        ex_refs = refs[n_in:n_in + len(ex_in)] + refs[n_in + 2 + len(ex_in):n_in + 2 + len(ex_in) + len(ex_out_shape)] + refs[-2:]
        grp, kj = pl.program_id(0), pl.program_id(1)

        def head_index(pp, e):
            return 2 * (pg * grp + pp) + e

        if exchange is not None:
            @pl.when((grp == 0) & (kj == 0))
            def _():
                exchange.start(*exchange.split(ex_refs))

        @pl.when(kj == 0)
        def _():
            gt = g_ref[...]
            for pp in range(pg):
                qv = qkv_refs[3 * pp][...]
                for e in range(2):
                    qa_ref[2 * pp + e] = _fox_operand(qv, e, _lane_col(gt, head_index(pp, e)), True)
            dq_ref[...] = jnp.zeros_like(dq_ref)

        @pl.when((grp == 0) & (kj == 0))
        def _():
            dc_ref[...] = jnp.zeros_like(dc_ref)

        masks = [_head_mask(0), _head_mask(1)]
        krows = pl.ds(pl.multiple_of(kj * BQ, BQ), BQ)
        gk = g_ref[krows, :]
        kas = [_fox_operand(qkv_refs[3 * pp + 1][...], e, _lane_col(gk, head_index(pp, e)), False) for pp, e in heads]
        vbs = [qkv_refs[3 * pp + 2][...].astype(BF16) for pp in range(pg)]
        lane = lax.broadcasted_iota(jnp.int32, (BQ, LANES), 1)
        n = range(len(heads))

        def block(qi, carry, diagonal):
            dks, dvs, css = carry
            rows = pl.ds(pl.multiple_of(qi * BQ, BQ), BQ)
            qa = [qa_ref[i, rows, :] for i in n]
            s = [_dot(qa[i], kas[i], 1, 1) for i in n]
            if diagonal:
                s = [jnp.where(_causal_block(), s[i], -jnp.inf) for i in n]
            dov = [do_ref[rows, pp * LANES:(pp + 1) * LANES] for pp in range(pg)]
            doe = [jnp.where(masks[e], dov[pp], jnp.zeros_like(dov[pp])) for pp, e in heads]
            lse2 = [lse_ref[pp, rows, :] for pp in range(pg)]
            dl2 = [dl_ref[pp, rows, :] for pp in range(pg)]
            p = [jnp.exp(s[i] - _lane_col(lse2[heads[i][0]], heads[i][1])) for i in n]
            dp = [_dot(doe[i], vbs[heads[i][0]], 1, 1) for i in n]
            ds = [p[i] * (dp[i] - _lane_col(dl2[heads[i][0]], heads[i][1])) for i in n]
            dv_part = [_dot(p[i], doe[i], 0, 0) for i in n]
            dk_part = [_dot(ds[i], jnp.where(masks[heads[i][1]], qa[i], jnp.zeros_like(qa[i])), 0, 0) for i in n]
            dq_part = [jnp.where(masks[heads[i][1]], _dot(ds[i], kas[i]), 0.0) for i in n]
            css = tuple(css[i] + jnp.sum(ds[i], axis=0, keepdims=True) for i in n)
            dc = jnp.zeros((BQ, LANES), F32)
            for i in n:
                dc = dc + jnp.where(lane == head_index(*heads[i]), jnp.sum(ds[i], axis=1, keepdims=True), 0.0)
            for pp in range(pg):
                dq_ref[pp, rows, :] += (dq_part[2 * pp] + dq_part[2 * pp + 1]) * FOX_SCALE
            dc_ref[rows, :] += dc
            dks = tuple(dks[pp] + dk_part[2 * pp] + dk_part[2 * pp + 1] for pp in range(pg))
            dvs = tuple(dvs[pp] + dv_part[2 * pp] + dv_part[2 * pp + 1] for pp in range(pg))
            return dks, dvs, css

        zero = jnp.zeros((BQ, LANES), F32)
        first = block(kj, ((zero,) * pg, (zero,) * pg, (jnp.zeros((1, BQ), F32),) * len(heads)), True)
        dks, dvs, css = lax.fori_loop(kj + 1, NQ, lambda qi, carry: block(qi, carry, False), first)
        r = lax.broadcasted_iota(jnp.int32, (BQ, BQ), 0)
        c = lax.broadcasted_iota(jnp.int32, (BQ, BQ), 1)
        dcol = jnp.zeros((BQ, LANES), F32)
        for i in n:
            col = jnp.sum(jnp.where(r == c, css[i], 0.0), axis=1, keepdims=True)
            dcol = dcol + jnp.where(lane == head_index(*heads[i]), col, 0.0)
        dc_ref[krows, :] -= dcol
        for pp in range(pg):
            base = 3 * pp * LANES
            dproj_ref[krows, base + LANES:base + 2 * LANES] = dks[pp].astype(BF16)
            dproj_ref[krows, base + 2 * LANES:base + 3 * LANES] = dvs[pp].astype(BF16)

        @pl.when(kj == NQ - 1)
        def _():
            for pp in range(pg):
                dproj_ref[:, 3 * pp * LANES:(3 * pp + 1) * LANES] = dq_ref[pp].astype(BF16)

        if exchange is not None:
            @pl.when((grp == NPAIR // pg - 1) & (kj == NQ - 1))
            def _():
                exchange.finish(*exchange.split(ex_refs))

    qkv_specs = []
    for pp in range(pg):
        qkv_specs.append(pl.BlockSpec((S, LANES), lambda g, j, pp=pp: (0, 3 * (pg * g + pp))))
        qkv_specs.append(pl.BlockSpec((BQ, LANES), lambda g, j, pp=pp: (j, 3 * (pg * g + pp) + 1)))
        qkv_specs.append(pl.BlockSpec((BQ, LANES), lambda g, j, pp=pp: (j, 3 * (pg * g + pp) + 2)))
    pair = pl.BlockSpec((pg, S, 2), lambda g, j: (g, 0, 0))
    res = pl.pallas_call(
        body, name="fox_bwd", grid=(NPAIR // pg, NQ),
        in_specs=qkv_specs + [pl.BlockSpec((S, pg * LANES), lambda g, j: (0, g)), pl.BlockSpec((S, LANES), lambda g, j: (0, 0)),
                              pair, pair] + ex_in_specs,
        out_specs=[pl.BlockSpec((S, 3 * pg * LANES), lambda g, j: (0, g)), pl.BlockSpec((S, LANES), lambda g, j: (0, 0))]
        + ex_out_specs,
        out_shape=[jax.ShapeDtypeStruct((S, DPROJ_PAD), BF16), jax.ShapeDtypeStruct((S, LANES), F32)] + ex_out_shape,
        scratch_shapes=[pltpu.VMEM((2 * pg, S, LANES), BF16), pltpu.VMEM((pg, S, LANES), F32)] + ex_scratch,
        compiler_params=_cparams(),
    )(*([proj] * (3 * pg)), do, gates, lse, delta, *ex_in)
    return res[0], res[1], res[2:]


NQKV = 3 * NGH
GDN_QSCALE = GHD ** -0.5


def _shift_down(x, s):
    if s == 0:
        return x
    row = lax.broadcasted_iota(jnp.int32, x.shape, 0)
    return jnp.where(row >= s, pltpu.roll(x, s, 0), 0.0)


def _shift_up(x, s):
    if s == 0:
        return x
    n = x.shape[0]
    row = lax.broadcasted_iota(jnp.int32, x.shape, 0)
    return jnp.where(row < n - s, pltpu.roll(x, n - s, 0), 0.0)


def _conv_pre(xv, wv):
    pre = xv * wv[CONV_K - 1:CONV_K, :]
    for j in range(CONV_K - 1):
        pre = pre + _shift_down(xv, CONV_K - 1 - j) * wv[j:j + 1, :]
    return pre


def _l2_factors(b):
    return b < 2 * NGH, jnp.where(b < NGH, GDN_QSCALE, 1.0)


def _gdn_pre(proj, conv_w):
    def body(x_ref, w_ref, o_ref):
        b = pl.program_id(0)
        c = _silu(_conv_pre(x_ref[...], w_ref[...]))
        normed, scale = _l2_factors(b)
        rs = lax.rsqrt(jnp.sum(c * c, axis=-1, keepdims=True) + EPS)
        o_ref[...] = c * jnp.where(normed, rs, 1.0) * scale

    return pl.pallas_call(
        body, name="gdn_pre", grid=(NQKV,),
        in_specs=[pl.BlockSpec((S, GHD), lambda b: (0, BLK_GDN + b)), pl.BlockSpec((CONV_K, GHD), lambda b: (0, b))],
        out_specs=pl.BlockSpec((S, GHD), lambda b: (0, b)),
        out_shape=jax.ShapeDtypeStruct((S, NQKV * GHD), F32), compiler_params=_cparams(),
    )(proj, conv_w)


def _gdn_pre_bwd(proj, conv_w, dqkv, dproj):
    def body(x_ref, w_ref, dy_ref, dproj_in, dx_ref, dw_ref):
        del dproj_in
        b = pl.program_id(0)
        xv = x_ref[...]
        wv = w_ref[...]
        pre = _conv_pre(xv, wv)
        sig = _sigmoid(pre)
        c = pre * sig
        normed, scale = _l2_factors(b)
        g = dy_ref[...] * scale
        rs = lax.rsqrt(jnp.sum(c * c, axis=-1, keepdims=True) + EPS)
        dc_n = rs * g - c * (rs * rs * rs) * jnp.sum(g * c, axis=-1, keepdims=True)
        dc = jnp.where(normed, dc_n, g)
        dpre = dc * sig * (1.0 + pre * (1.0 - sig))
        dx = dpre * wv[CONV_K - 1:CONV_K, :]
        for j in range(CONV_K - 1):
            dx = dx + _shift_up(dpre, CONV_K - 1 - j) * wv[j:j + 1, :]
        dx_ref[...] = dx.astype(BF16)
        for j in range(CONV_K):
            dw_ref[j:j + 1, :] = jnp.sum(dpre * _shift_down(xv, CONV_K - 1 - j), axis=0, keepdims=True)

    return pl.pallas_call(
        body, name="gdn_pre_bwd", grid=(NQKV,),
        in_specs=[pl.BlockSpec((S, GHD), lambda b: (0, BLK_GDN + b)), pl.BlockSpec((CONV_K, GHD), lambda b: (0, b)),
                  pl.BlockSpec((None, S, GHD), lambda b: (b // NGH, 0, b % NGH)), pl.BlockSpec(memory_space=pl.ANY)],
        out_specs=[pl.BlockSpec((S, GHD), lambda b: (0, BLK_GDN + b)), pl.BlockSpec((CONV_K, GHD), lambda b: (0, b))],
        out_shape=[jax.ShapeDtypeStruct((S, DPROJ_PAD), BF16), jax.ShapeDtypeStruct((CONV_K, NQKV * GHD), F32)],
        input_output_aliases={3: 0}, compiler_params=_cparams(),
    )(proj, conv_w, dqkv, dproj)


CB = 4
NCB = NCH // CB


def _chunk_prep(qs, ks, vs, gcols, bcols, t_saved=None):
    n = range(len(qs))
    r = lax.broadcasted_iota(jnp.int32, (CHUNK, CHUNK), 0)
    c = lax.broadcasted_iota(jnp.int32, (CHUNK, CHUNK), 1)
    incl = c <= r
    eye = (r == c).astype(F32)
    grow = [jnp.sum(gcols[i] * eye, axis=0, keepdims=True) for i in n]
    gc_col = [jnp.sum(jnp.where(incl, grow[i], 0.0), axis=1, keepdims=True) for i in n]
    gc_row = [jnp.sum(jnp.where(r <= c, gcols[i], 0.0), axis=0, keepdims=True) for i in n]
    decay = [jnp.exp(jnp.where(incl, gc_col[i] - gc_row[i], -jnp.inf)) for i in n]
    kb = [ks[i] * bcols[i] for i in n]
    vb = [vs[i] * bcols[i] for i in n]
    kk = [_mm_nt(kb[i], ks[i]) for i in n]
    m = [jnp.where(c < r, kk[i] * decay[i], 0.0) for i in n]
    if t_saved is None:
        t_inv = [eye - m[i] for i in n]
        p = [_dot3(m[i], m[i]) for i in n]
        for step in range(5):
            t_inv = [t_inv[i] + _dot3(t_inv[i], p[i]) for i in n]
            if step < 4:
                p = [_dot3(p[i], p[i]) for i in n]
    else:
        t_inv = [_saved_inverse(m[i], t_saved[i]) for i in n]
    egc = [jnp.exp(gc_col[i]) for i in n]
    u = [_mm_nn(t_inv[i], vb[i]) for i in n]
    w = [_mm_nn(t_inv[i], kb[i] * egc[i]) for i in n]
    qk = [_mm_nt(qs[i], ks[i]) for i in n]
    gc_last = [gc_col[i][CHUNK - 1:CHUNK, :] for i in n]
    return [(u[i], w[i], qk[i] * decay[i], qs[i] * egc[i], ks[i] * jnp.exp(gc_last[i] - gc_col[i]), jnp.exp(gc_last[i]),
             t_inv[i]) for i in n]


def _prep_specs():
    rows = CB * CHUNK
    qs = pl.BlockSpec((rows, GHD), lambda i, h: (i, h))
    ks = pl.BlockSpec((rows, GHD), lambda i, h: (i, NGH + h))
    vs = pl.BlockSpec((rows, GHD), lambda i, h: (i, 2 * NGH + h))
    gs = pl.BlockSpec((rows, LANES), lambda i, h: (i, 0))
    a_s = pl.BlockSpec((None, rows, CHUNK), lambda i, h: (h, i, 0))
    gl_s = pl.BlockSpec((None, CB, 1, LANES), lambda i, h: (h, i, 0, 0))
    return qs, ks, vs, gs, a_s, gl_s


def _gdn_prep(qkv, gates, exchange=None):
    ex_in, ex_in_specs, ex_out_specs, ex_out_shape, ex_scratch = _hosted(exchange)

    def body(*refs):
        q_ref, k_ref, v_ref, g_ref = refs[:4]
        u_ref, w_ref, qd_ref, kd_ref, a_ref, gl_ref, t_ref = refs[4 + len(ex_in):11 + len(ex_in)]
        ex_refs = refs[4:4 + len(ex_in)] + refs[11 + len(ex_in):]
        h = pl.program_id(1)

        if exchange is not None:
            @pl.when((pl.program_id(0) == 0) & (h == 0))
            def _():
                exchange.start(*exchange.split(ex_refs))

        chunks = [pl.ds(cidx * CHUNK, CHUNK) for cidx in range(CB)]
        gts = [g_ref[rows, :] for rows in chunks]
        outs = _chunk_prep([q_ref[rows, :] for rows in chunks], [k_ref[rows, :] for rows in chunks],
                           [v_ref[rows, :] for rows in chunks], [_lane_col(gt, LANE_G + h) for gt in gts],
                           [_lane_col(gt, LANE_BETA + h) for gt in gts])
        for cidx, rows in enumerate(chunks):
            u, w, a, qd, kd, gl, t_inv = outs[cidx]
            u_ref[rows, :] = u
            w_ref[rows, :] = w
            qd_ref[rows, :] = qd
            kd_ref[rows, :] = kd
            a_ref[rows, :] = a
            t_ref[rows, :] = t_inv
            gl_ref[cidx] = jnp.broadcast_to(gl, (1, LANES))

        if exchange is not None:
            @pl.when((pl.program_id(0) == NCB - 1) & (h == NGH - 1))
            def _():
                exchange.finish(*exchange.split(ex_refs))

    qs, ks, vs, gs, a_s, gl_s = _prep_specs()
    tok = jax.ShapeDtypeStruct((S, DGDN), F32)
    sq = jax.ShapeDtypeStruct((NGH, S, CHUNK), F32)
    res = pl.pallas_call(
        body, name="gdn_prep", grid=(NCB, NGH), in_specs=[qs, ks, vs, gs] + ex_in_specs,
        out_specs=[qs, qs, qs, qs, a_s, gl_s, a_s] + ex_out_specs,
        out_shape=[tok, tok, tok, tok, sq, jax.ShapeDtypeStruct((NGH, NCH, 1, LANES), F32), sq] + ex_out_shape,
        scratch_shapes=ex_scratch, compiler_params=_cparams(),
    )(qkv, qkv, qkv, gates, *ex_in)
    return res[:7], res[7:]


def _gdn_prep_bwd(qkv, gates, t_inv, du, dw, dqd, dkd, da, dgl):
    def body(q_ref, k_ref, v_ref, g_ref, t_ref, du_ref, dw_ref, dqd_ref, dkd_ref, da_ref, dgl_ref, dqkv_ref, dg_ref):
        h = pl.program_id(1)

        @pl.when(h == 0)
        def _():
            dg_ref[...] = jnp.zeros_like(dg_ref)

        lane = lax.broadcasted_iota(jnp.int32, (CHUNK, LANES), 1)
        chunks = [pl.ds(cidx * CHUNK, CHUNK) for cidx in range(CB)]
        gts = [g_ref[rows, :] for rows in chunks]
        t_saved = [t_ref[rows, :] for rows in chunks]
        _, vjp = jax.vjp(lambda *args: [o[:6] for o in _chunk_prep(*args, t_saved=t_saved)],
                         [q_ref[rows, :] for rows in chunks], [k_ref[rows, :] for rows in chunks],
                         [v_ref[rows, :] for rows in chunks], [_lane_col(gt, LANE_G + h) for gt in gts],
                         [_lane_col(gt, LANE_BETA + h) for gt in gts])
        dqs, dks, dvs, dgcs, dbcs = vjp([(du_ref[rows, :], dw_ref[rows, :], da_ref[rows, :], dqd_ref[rows, :],
                                          dkd_ref[rows, :], dgl_ref[cidx][:, 0:1]) for cidx, rows in enumerate(chunks)])
        for cidx, rows in enumerate(chunks):
            dq, dk, dv, dgc, dbc = dqs[cidx], dks[cidx], dvs[cidx], dgcs[cidx], dbcs[cidx]
            dqkv_ref[0, rows, :] = dq
            dqkv_ref[1, rows, :] = dk
            dqkv_ref[2, rows, :] = dv
            dg_ref[rows, :] += jnp.where(lane == LANE_G + h, dgc, 0.0) + jnp.where(lane == LANE_BETA + h, dbc, 0.0)

    qs, ks, vs, gs, a_s, gl_s = _prep_specs()
    return pl.pallas_call(
        body, name="gdn_prep_bwd", grid=(NCB, NGH), in_specs=[qs, ks, vs, gs, a_s, qs, qs, qs, qs, a_s, gl_s],
        out_specs=[pl.BlockSpec((3, CB * CHUNK, GHD), lambda i, h: (0, i, h)), gs],
        out_shape=[jax.ShapeDtypeStruct((3, S, DGDN), F32), jax.ShapeDtypeStruct((S, LANES), F32)],
        compiler_params=_cparams(),
    )(qkv, qkv, qkv, gates, t_inv, du, dw, dqd, dkd, da, dgl)


SCAN_HEADS = 2


def _scan_specs():
    wide = SCAN_HEADS * GHD
    hs = pl.BlockSpec((S, wide), lambda g: (0, g))
    a_s = pl.BlockSpec((SCAN_HEADS, S, CHUNK), lambda g: (g, 0, 0))
    gl_s = pl.BlockSpec((SCAN_HEADS, NCH, 1, LANES), lambda g: (g, 0, 0, 0))
    st_s = pl.BlockSpec((SCAN_HEADS, NCH, GHD, GHD), lambda g: (g, 0, 0, 0))
    gz_s = pl.BlockSpec((S, wide), lambda g: (0, BLK_GZ // SCAN_HEADS + g))
    mix_s = pl.BlockSpec((S, wide), lambda g: (0, NPAIR // SCAN_HEADS + g))
    return hs, a_s, gl_s, st_s, gz_s, mix_s


def _head_cols(hh):
    return slice(hh * GHD, (hh + 1) * GHD)


def _gdn_scan(u, w, qd, kd, a, gl, proj, w_norm, mix):
    heads = range(SCAN_HEADS)

    def body(u_ref, w_ref, qd_ref, kd_ref, a_ref, gl_ref, z_ref, wn_ref, mix_in, mix_ref, o_ref, st_ref):
        del mix_in

        def step(ci, states):
            rows = pl.ds(pl.multiple_of(ci * CHUNK, CHUNK), CHUNK)
            for hh in heads:
                st_ref[hh, ci] = states[hh]
            ws = [_dot(w_ref[rows, _head_cols(hh)], states[hh]) for hh in heads]
            qs = [_dot(qd_ref[rows, _head_cols(hh)], states[hh]) for hh in heads]
            vn = [u_ref[rows, _head_cols(hh)] - ws[hh] for hh in heads]
            av = [_dot(a_ref[hh, rows, :], vn[hh]) for hh in heads]
            kv = [_dot(kd_ref[rows, _head_cols(hh)], vn[hh], 0, 0) for hh in heads]
            for hh in heads:
                o_ref[rows, _head_cols(hh)] = qs[hh] + av[hh]
            return tuple(states[hh] * gl_ref[hh, ci] + kv[hh] for hh in heads)

        lax.fori_loop(0, NCH, step, (jnp.zeros((GHD, GHD), F32),) * SCAN_HEADS)
        for hh in heads:
            ov = o_ref[:, _head_cols(hh)]
            mix_ref[:, _head_cols(hh)] = (ov * _rms_scale(ov) * wn_ref[...] * _silu(z_ref[:, _head_cols(hh)])).astype(BF16)

    hs, a_s, gl_s, st_s, gz_s, mix_s = _scan_specs()
    return pl.pallas_call(
        body, name="gdn_scan", grid=(NGH // SCAN_HEADS,),
        in_specs=[hs, hs, hs, hs, a_s, gl_s, gz_s, pl.BlockSpec((1, GHD), lambda h: (0, 0)), pl.BlockSpec(memory_space=pl.ANY)],
        out_specs=[mix_s, hs, st_s],
        out_shape=[jax.ShapeDtypeStruct((S, D), BF16), jax.ShapeDtypeStruct((S, DGDN), F32),
                   jax.ShapeDtypeStruct((NGH, NCH, GHD, GHD), F32)],
        input_output_aliases={8: 0}, compiler_params=_cparams(),
    )(u, w, qd, kd, a, gl, proj, w_norm, mix)


def _gdn_scan_bwd(dmix, o, proj, w_norm, u, w, qd, kd, a, gl, states, dproj):
    def body(dy_ref, o_ref, z_ref, wn_ref, u_ref, w_ref, qd_ref, kd_ref, a_ref, gl_ref, st_ref, dproj_in,
             dz_ref, du_ref, dw_ref, dqd_ref, dkd_ref, da_ref, dgl_ref, dwn_ref, do_ref):
        del dproj_in
        heads = range(SCAN_HEADS)

        @pl.when(pl.program_id(0) == 0)
        def _():
            dwn_ref[...] = jnp.zeros_like(dwn_ref)

        wn = wn_ref[...]
        for hh in heads:
            c = _head_cols(hh)
            ov = o_ref[:, c]
            zv = z_ref[:, c]
            g = dy_ref[:, c]
            sig = _sigmoid(zv)
            dz_ref[:, c] = (g * (ov * _rms_scale(ov) * wn) * sig * (1.0 + zv * (1.0 - sig))).astype(BF16)
            do, dwt = _rms_bwd(ov, wn, g * zv * sig)
            do_ref[:, c] = do
            dwn_ref[...] += jnp.sum(dwt, axis=0, keepdims=True)

        def step(t, dstates):
            ci = NCH - 1 - t
            rows = pl.ds(pl.multiple_of(ci * CHUNK, CHUNK), CHUNK)
            cols = [_head_cols(hh) for hh in heads]
            state = [st_ref[hh, ci] for hh in heads]
            dov = [do_ref[rows, cols[hh]] for hh in heads]
            wv = [w_ref[rows, cols[hh]] for hh in heads]
            ws = [_dot(wv[hh], state[hh]) for hh in heads]
            adov = [_dot(a_ref[hh, rows, :], dov[hh], 0, 0) for hh in heads]
            kds = [_dot(kd_ref[rows, cols[hh]], dstates[hh]) for hh in heads]
            dqd = [_dot(dov[hh], state[hh], 1, 1) for hh in heads]
            qdo = [_dot(qd_ref[rows, cols[hh]], dov[hh], 0, 0) for hh in heads]
            vn = [u_ref[rows, cols[hh]] - ws[hh] for hh in heads]
            dvn = [adov[hh] + kds[hh] for hh in heads]
            da = [_dot(dov[hh], vn[hh], 1, 1) for hh in heads]
            dkd = [_dot(vn[hh], dstates[hh], 1, 1) for hh in heads]
            dwv = [_dot(dvn[hh], state[hh], 1, 1) for hh in heads]
            wdv = [_dot(wv[hh], dvn[hh], 0, 0) for hh in heads]
            for hh in heads:
                da_ref[hh, rows, :] = da[hh]
                dqd_ref[rows, cols[hh]] = dqd[hh]
                dkd_ref[rows, cols[hh]] = dkd[hh]
                dgl = jnp.sum(jnp.sum(dstates[hh] * state[hh], axis=1, keepdims=True), axis=0, keepdims=True)
                dgl_ref[hh, ci] = jnp.broadcast_to(dgl, (1, LANES))
                du_ref[rows, cols[hh]] = dvn[hh]
                dw_ref[rows, cols[hh]] = -dwv[hh]
            return tuple(dstates[hh] * gl_ref[hh, ci] + qdo[hh] - wdv[hh] for hh in heads)

        lax.fori_loop(0, NCH, step, (jnp.zeros((GHD, GHD), F32),) * SCAN_HEADS)

    hs, a_s, gl_s, st_s, gz_s, mix_s = _scan_specs()
    once = pl.Buffered(buffer_count=1)
    hs1, a1, st1 = [pl.BlockSpec(s.block_shape, s.index_map, pipeline_mode=once) for s in (hs, a_s, st_s)]
    vec = pl.BlockSpec((1, GHD), lambda h: (0, 0))
    tok = jax.ShapeDtypeStruct((S, DGDN), F32)
    return pl.pallas_call(
        body, name="gdn_scan_bwd", grid=(NGH // SCAN_HEADS,),
        in_specs=[mix_s, hs, gz_s, vec, hs, hs, hs, hs, a1, gl_s, st1, pl.BlockSpec(memory_space=pl.ANY)],
        out_specs=[gz_s, hs1, hs1, hs1, hs1, a1, gl_s, vec],
        out_shape=[jax.ShapeDtypeStruct((S, DPROJ_PAD), BF16), tok, tok, tok, tok,
                   jax.ShapeDtypeStruct((NGH, S, CHUNK), F32), jax.ShapeDtypeStruct((NGH, NCH, 1, LANES), F32),
                   jax.ShapeDtypeStruct((1, GHD), F32)],
        input_output_aliases={11: 0}, scratch_shapes=[pltpu.VMEM((S, SCAN_HEADS * GHD), F32)], compiler_params=_cparams(),
    )(dmix, o, proj, w_norm, u, w, qd, kd, a, gl, states, dproj)


def _place():
    return lax.axis_index("x"), lax.axis_index("y"), lax.axis_index("c")


def _other_chips(x, y):
    return [(1 - x, y), (x, 1 - y), (1 - x, 1 - y)]


HBM = pl.BlockSpec(memory_space=pltpu.HBM)
VMEM = pl.BlockSpec(memory_space=pltpu.VMEM)


def _half_rows(ref_or_rows, half):
    rows = ref_or_rows // 2
    return pl.ds(pl.multiple_of(half * rows, rows), rows)


class _Exchange:
    def __init__(self, inputs, out_shape, n_sems, start, finish):
        self.inputs, self.out_shape, self.n_sems, self.start, self.finish = inputs, out_shape, n_sems, start, finish

    def sem_shapes(self):
        return [pltpu.SemaphoreType.DMA((self.n_sems,)), pltpu.SemaphoreType.DMA((self.n_sems,))]

    def split(self, refs):
        n_in, n_out = len(self.inputs), len(self.out_shape)
        return refs[:n_in], refs[n_in:n_in + n_out], refs[n_in + n_out], refs[n_in + n_out + 1]


def _run_exchange(ex, name):
    def body(*refs):
        parts = ex.split(refs)
        ex.start(*parts)
        ex.finish(*parts)

    return pl.pallas_call(
        body, name=name, in_specs=[HBM] * len(ex.inputs), out_specs=[HBM] * len(ex.out_shape), out_shape=ex.out_shape,
        scratch_shapes=ex.sem_shapes(), compiler_params=_cparams(),
    )(*ex.inputs)


def _allgather_exchange(shards, whole=()):
    n, nw = len(shards), len(whole)

    def plan(src, outs, send_sems, recv_sems):
        x, y, c = _place()
        chips = _other_chips(x, y)
        chip_ids = [2 * ch[0] + ch[1] for ch in chips]

        def copy(a, k, chip_index, half, to, from_src):
            rows = _half_rows(src[a].shape[0], half)
            dst = outs[a].at[chip_index, rows]
            return pltpu.make_async_remote_copy(
                src_ref=src[a].at[rows] if from_src else dst, dst_ref=dst, send_sem=send_sems.at[6 * a + k],
                recv_sem=recv_sems.at[6 * a + k], device_id=to, device_id_type=MESH)

        def whole_copy(b, k, chip_index, to):
            return pltpu.make_async_remote_copy(
                src_ref=src[n + b], dst_ref=outs[n + b].at[chip_index], send_sem=send_sems.at[6 * n + 3 * b + k],
                recv_sem=recv_sems.at[6 * n + 3 * b + k], device_id=to, device_id_type=MESH)

        me, sibling = (x, y, c), (x, y, 1 - c)
        first = [copy(a, j, 2 * x + y, c, (*chips[j], c), True) for a in range(n) for j in range(3)]
        first += [whole_copy(b, j, 2 * x + y, (*chips[j], c)) for b in range(nw) for j in range(3)]
        landing = [copy(a, j, chip_ids[j], c, me, False) for a in range(n) for j in range(3)]
        passed = [copy(a, 3 + j, chip_ids[j], c, sibling, False) for a in range(n) for j in range(3)]
        arriving = [copy(a, 3 + j, chip_ids[j], 1 - c, me, False) for a in range(n) for j in range(3)]
        arriving += [whole_copy(b, j, chip_ids[j], me) for b in range(nw) for j in range(3)]
        return first, landing, passed, arriving

    def start(*refs):
        for cp in plan(*refs)[0]:
            cp.start()

    def finish(*refs):
        first, landing, passed, arriving = plan(*refs)
        for lands, onward in zip(landing, passed):
            lands.wait_recv()
            onward.start()
        for cp in arriving:
            cp.wait_recv()
        for cp in first + passed:
            cp.wait_send()

    out_shape = [jax.ShapeDtypeStruct((NCHIP,) + s.shape, s.dtype) for s in list(shards) + list(whole)]
    return _Exchange(list(shards) + list(whole), out_shape, 6 * n + 3 * nw, start, finish)


def _with_own(gathered, own):
    x, y, _ = _place()
    return lax.dynamic_update_index_in_dim(gathered, own, 2 * x + y, axis=0)


def _simple_exchange(inputs, out_shape, copies_of):
    def start(*refs):
        for cp in copies_of(*refs):
            cp.start()

    def finish(*refs):
        for cp in copies_of(*refs):
            cp.wait()

    return _Exchange(list(inputs), out_shape, len(out_shape) * 3, start, finish)


def _pair_exchange(grads):
    def copies_of(src, outs, send_sems, recv_sems):
        x, y, c = _place()
        return [pltpu.make_async_remote_copy(
            src_ref=src[a].at[:, _half_rows(src[a].shape[1], 1 - c)], dst_ref=outs[a], send_sem=send_sems.at[a],
            recv_sem=recv_sems.at[a], device_id=(x, y, 1 - c), device_id_type=MESH) for a in range(len(src))]

    return _simple_exchange(
        grads, [jax.ShapeDtypeStruct((g.shape[0], g.shape[1] // 2, g.shape[2]), g.dtype) for g in grads], copies_of)


def _pair_sum(grads, theirs, name):
    n = len(grads)

    def body(*refs):
        south = lax.axis_index("c") == 0
        for a in range(n):
            g = refs[a][...]
            half = g.shape[0] // 2
            mine = jnp.where(south, g[:half], g[half:])
            refs[2 * n + a][...] = (mine.astype(F32) + refs[n + a][...].astype(F32)).astype(BF16)

    def specs(arrs):
        return [pl.BlockSpec((None,) + g.shape[1:], lambda j: (j, 0, 0)) for g in arrs]

    return pl.pallas_call(
        body, name=name, grid=(NCHIP,), in_specs=specs(grads) + specs(theirs), out_specs=specs(theirs),
        out_shape=[jax.ShapeDtypeStruct(g.shape, BF16) for g in theirs], compiler_params=_cparams(),
    )(*grads, *theirs)


def _chip_exchange(parts):
    def copies_of(src, outs, send_sems, recv_sems):
        x, y, c = _place()
        return [pltpu.make_async_remote_copy(
            src_ref=src[a].at[2 * chip[0] + chip[1]], dst_ref=outs[a].at[k], send_sem=send_sems.at[3 * a + k],
            recv_sem=recv_sems.at[3 * a + k], device_id=(*chip, c), device_id_type=MESH)
            for a in range(len(src)) for k, chip in enumerate(_other_chips(x, y))]

    return _simple_exchange(parts, [jax.ShapeDtypeStruct((NCHIP - 1,) + p.shape[1:], p.dtype) for p in parts], copies_of)


def _chip_sum(parts, received):
    n = len(parts)
    steps = 4

    def body(*refs):
        chip = 2 * lax.axis_index("x") + lax.axis_index("y")
        for a in range(n):
            p, r = refs[a], refs[n + a]
            own = jnp.where(chip == 0, p[0], jnp.where(chip == 1, p[1], jnp.where(chip == 2, p[2], p[3])))
            refs[2 * n + a][...] = ((own.astype(F32) + r[0].astype(F32)) + r[1].astype(F32)) + r[2].astype(F32)

    def specs(arrs):
        return [pl.BlockSpec((g.shape[0], g.shape[1] // steps, g.shape[2]), lambda i: (0, i, 0)) for g in arrs]

    out_specs = [pl.BlockSpec((g.shape[1] // steps, g.shape[2]), lambda i: (i, 0)) for g in parts]
    return pl.pallas_call(
        body, name="grads_chip_sum", grid=(steps,), in_specs=specs(parts) + specs(received), out_specs=out_specs,
        out_shape=[jax.ShapeDtypeStruct(g.shape[1:], F32) for g in parts], compiler_params=_cparams(),
    )(*parts, *received)


def _pair_share(halves):
    def copies_of(src, outs, send_sems, recv_sems):
        x, y, c = _place()
        return [pltpu.make_async_remote_copy(
            src_ref=src[a], dst_ref=outs[a], send_sem=send_sems.at[a], recv_sem=recv_sems.at[a],
            device_id=(x, y, 1 - c), device_id_type=MESH) for a in range(len(src))]

    return _simple_exchange(halves, [jax.ShapeDtypeStruct(h.shape, F32) for h in halves], copies_of)


def _adamw_math(w, g, m, v):
    nm = ADAM_B1 * m + (1.0 - ADAM_B1) * g
    nv = ADAM_B2 * v + (1.0 - ADAM_B2) * jnp.square(g)
    m_hat = nm / (1.0 - ADAM_B1 ** ADAM_STEP)
    v_hat = nv / (1.0 - ADAM_B2 ** ADAM_STEP)
    return -ADAM_LR * (m_hat / (jnp.sqrt(v_hat) + ADAM_EPS) + ADAM_WD * w), nm, nv


def _adamw_big(ws, g_mine, g_theirs, ms, vs, exchange=None):
    n = len(ws)
    steps = 8
    ex_in, ex_in_specs, ex_out_specs, ex_out_shape, ex_scratch = _hosted(exchange)

    def body(*refs):
        ex_refs = refs[5 * n:5 * n + len(ex_in)] + refs[9 * n + len(ex_in):]
        outs = refs[5 * n + len(ex_in):9 * n + len(ex_in)]
        if exchange is not None:
            @pl.when(pl.program_id(0) == 0)
            def _():
                exchange.start(*exchange.split(ex_refs))

        own_half = (pl.program_id(0) // (steps // 2)) == lax.axis_index("c")
        for a in range(n):
            g = jnp.where(own_half, refs[n + a][...], refs[2 * n + a][...])
            d, nm, nv = _adamw_math(refs[a][...], g, refs[3 * n + a][...], refs[4 * n + a][...])
            outs[a][...] = g
            outs[n + a][...] = d
            outs[2 * n + a][...] = nm
            outs[3 * n + a][...] = nv

        if exchange is not None:
            @pl.when(pl.program_id(0) == steps - 1)
            def _():
                exchange.finish(*exchange.split(ex_refs))

    specs = [pl.BlockSpec((w.shape[0] // steps, w.shape[1]), lambda i: (i, 0)) for w in ws]
    half_specs = [pl.BlockSpec((g.shape[0] // (steps // 2), g.shape[1]), lambda i: (i % (steps // 2), 0)) for g in g_mine]
    shapes = [jax.ShapeDtypeStruct(w.shape, F32) for w in ws]
    res = pl.pallas_call(
        body, name="adamw_big", grid=(steps,), in_specs=specs + half_specs * 2 + specs * 2 + ex_in_specs,
        out_specs=specs * 4 + ex_out_specs, out_shape=shapes * 4 + ex_out_shape, scratch_shapes=ex_scratch,
        compiler_params=_cparams(),
    )(*ws, *g_mine, *g_theirs, *ms, *vs, *ex_in)
    return res[:n], res[n:2 * n], res[2 * n:3 * n], res[3 * n:4 * n], res[4 * n:]


def _adamw_in(w, g_mine, g_theirs, m, v):
    half = D // 2

    def body(w_ref, gm_ref, gt_ref, m_ref, v_ref, g_out, d_out, nm_out, nv_out, g_ref):
        south = lax.axis_index("c") == 0
        g_ref[0:half, :] = jnp.where(south, gm_ref[...], gt_ref[...])
        g_ref[half:D, :] = jnp.where(south, gt_ref[...], gm_ref[...])
        g = g_ref[0:CW, :]
        d, nm, nv = _adamw_math(w_ref[...], g, m_ref[...], v_ref[...])
        g_out[...] = g
        d_out[...] = d
        nm_out[...] = nm
        nv_out[...] = nv

    spec = pl.BlockSpec((CW, LANES), lambda i: (0, i))
    half_spec = pl.BlockSpec((half, LANES), lambda i: (0, i))
    return pl.pallas_call(
        body, name="adamw_in", grid=(D // LANES,), in_specs=[spec, half_spec, half_spec, spec, spec], out_specs=[spec] * 4,
        out_shape=[jax.ShapeDtypeStruct((CW, D), F32)] * 4, scratch_shapes=[pltpu.VMEM((D, LANES), F32)],
        compiler_params=_cparams(),
    )(w, g_mine, g_theirs, m, v)


NORM_NAMES = ("pre_mix_norm", "post_mix_norm", "pre_mlp_norm", "post_mlp_norm")
SMALL_NAMES = NORM_NAMES + ("gdn_conv_w", "fox_f_bias", "gdn_dt_bias", "gdn_a_log", "fox_out_norm", "gdn_out_norm")
CONV_COLS = 3 * DGDN // NCHIP


def _small_gather(d_norms, d_conv, sums, d_fox_norm, d_gdn_norm):
    n_remote = 5 * (NDEV - 1)

    def copies_of(src, outs, send_sems, recv_sems):
        x, y, c = _place()
        me = 4 * x + 2 * y + c

        def from_me(chip_index):
            cols = pl.ds(pl.multiple_of(chip_index * CONV_COLS, LANES), CONV_COLS)
            return [src[0], src[1].at[:, cols], src[2], src[3], src[4]]

        local = [pltpu.make_async_copy(s, outs[a].at[me], send_sems.at[n_remote + a]) for a, s in enumerate(from_me(2 * x + y))]
        remote = []
        for k in range(1, NDEV):
            px, py, pc = x ^ ((k >> 2) & 1), y ^ ((k >> 1) & 1), c ^ (k & 1)
            remote += [pltpu.make_async_remote_copy(
                src_ref=s, dst_ref=outs[a].at[me], send_sem=send_sems.at[5 * (k - 1) + a],
                recv_sem=recv_sems.at[5 * (k - 1) + a], device_id=(px, py, pc), device_id_type=MESH)
                for a, s in enumerate(from_me(2 * px + py))]
        return local + remote

    def start(*refs):
        for cp in copies_of(*refs):
            cp.start()

    def finish(*refs):
        for cp in copies_of(*refs):
            cp.wait()

    shapes = [(4, D), (CONV_K, CONV_COLS), (8, LANES), (1, LANES), (1, LANES)]
    return _Exchange([d_norms, d_conv, sums, d_fox_norm, d_gdn_norm],
                     [jax.ShapeDtypeStruct((NDEV,) + s, F32) for s in shapes], n_remote + 5, start, finish)


def _small_adamw(gathered, ws, ms, vs):
    n = len(SMALL_NAMES)

    def body(*refs):
        def total(buf):
            acc = buf[0]
            for i in range(1, NDEV):
                acc = acc + buf[i]
            return acc

        t_norms, t_conv, t_sums, t_fn, t_gn = [total(r) for r in refs[:5]]
        w_refs, m_refs, v_refs = refs[5:5 + n], refs[5 + n:5 + 2 * n], refs[5 + 2 * n:5 + 3 * n]
        outs = refs[5 + 3 * n:]
        grads = [t_norms[i:i + 1, :] for i in range(4)] + [
            t_conv, t_sums[0:1, 0:NFH], t_sums[1:2, 0:NGH], t_sums[2:3, 0:NGH], t_fn[:, 0:FHD], t_gn]
        for a in range(n):
            d, nm, nv = _adamw_math(w_refs[a][...], grads[a], m_refs[a][...], v_refs[a][...])
            outs[a][...] = grads[a]
            outs[n + a][...] = d
            outs[2 * n + a][...] = nm
            outs[3 * n + a][...] = nv

    def whole(arr):
        return pl.BlockSpec(arr.shape, lambda i: (0,) * arr.ndim)

    res = pl.pallas_call(
        body, name="small_adamw", grid=(1,), in_specs=[whole(t) for t in gathered] + [whole(w) for w in ws] * 3,
        out_specs=[whole(w) for w in ws] * 4, out_shape=[jax.ShapeDtypeStruct(w.shape, F32) for w in ws] * 4,
        compiler_params=_cparams(),
    )(*gathered, *ws, *ms, *vs)
    return res[:n], res[n:2 * n], res[2 * n:3 * n], res[3 * n:]


CW = DPROJ // NCHIP
PROJ_RUNS = tuple((part * DFOX + hp * LANES, part * DFOX + (hp + 1) * LANES, (3 * hp + part) * LANES)
                  for hp in range(NPAIR) for part in range(3)) + (
    (1536, 1544, BLK_SMALL * LANES), (1544, 3080, BLK_GDN * LANES), (3080, 3088, BLK_SMALL * LANES + 8),
    (3088, 3600, BLK_GZ * LANES))


def _proj_pieces():
    pieces = []
    for lo, hi, at in PROJ_RUNS:
        while lo < hi:
            j = lo // CW
            end = min(hi, (j + 1) * CW)
            pieces.append((j, lo - j * CW, at, end - lo))
            at, lo = at + end - lo, end
    return pieces


RT = 256


def _to_padded_rows(gathered):
    def body(src_ref, out_ref, blocks_ref, rows_ref):
        blocks_ref[...] = src_ref[...].astype(F32)
        rows_ref[...] = jnp.zeros_like(rows_ref)
        for j, start, at, n in _proj_pieces():
            rows_ref[at:at + n, :] = blocks_ref[j, start:start + n, :]
        out_ref[...] = rows_ref[...].astype(out_ref.dtype)

    return pl.pallas_call(
        body, name="proj_rows_in", grid=(D // RT,), in_specs=[pl.BlockSpec((NCHIP, D, RT), lambda i: (0, 0, i))],
        out_specs=pl.BlockSpec((DPROJ_PAD, RT), lambda i: (0, i)), out_shape=jax.ShapeDtypeStruct((DPROJ_PAD, D), gathered.dtype),
        scratch_shapes=[pltpu.VMEM((NCHIP, D, RT), F32), pltpu.VMEM((DPROJ_PAD, RT), F32)], compiler_params=_cparams(),
    )(gathered)


def _from_padded_rows(w):
    def body(src_ref, out_ref, rows_ref, blocks_ref):
        rows_ref[...] = src_ref[...].astype(F32)
        blocks_ref[...] = jnp.zeros_like(blocks_ref)
        for j, start, at, n in _proj_pieces():
            blocks_ref[j, start:start + n, :] = rows_ref[at:at + n, :]
        out_ref[...] = blocks_ref[...].astype(out_ref.dtype)

    return pl.pallas_call(
        body, name="proj_rows_out", grid=(D // RT,), in_specs=[pl.BlockSpec((DPROJ_PAD, RT), lambda i: (0, i))],
        out_specs=pl.BlockSpec((NCHIP, D, RT), lambda i: (0, 0, i)), out_shape=jax.ShapeDtypeStruct((NCHIP, D, D), w.dtype),
        scratch_shapes=[pltpu.VMEM((DPROJ_PAD, RT), F32), pltpu.VMEM((NCHIP, D, RT), F32)], compiler_params=_cparams(),
    )(w)


def _local_step(x, target, first_weights, late_weights, reduce_late, reduce_in, pre_mix_norm, fox_f_bias, fox_out_norm,
                gdn_a_log, gdn_dt_bias, gdn_out_norm, post_mix_norm, pre_mlp_norm, post_mlp_norm):
    bias_vec = jnp.zeros((1, LANES), F32).at[0, 0:NFH].set(fox_f_bias).at[0, LANE_G:LANE_G + NGH].set(gdn_dt_bias)
    alog_vec = jnp.zeros((1, LANES), F32).at[0, LANE_G:LANE_G + NGH].set(gdn_a_log)
    w2 = jnp.concatenate([fox_out_norm, fox_out_norm], axis=1)

    h, first = _pre_norm(x, pre_mix_norm, exchange=first_weights[0])
    win_p, conv_w = first_weights[1](first)
    proj = _matmul(h, win_p, tb=True, tm=2048, tn=768, tk=1024, name="mm_proj")
    gates = _gates(proj, bias_vec, alog_vec)
    mix, fox_o, lse, late_a = _fox_fwd(proj, gates, w2, exchange=late_weights[0])
    qkv = _gdn_pre(proj, conv_w)
    (u, w, qd, kd, a_intra, gl, t_inv), late_b = _gdn_prep(qkv, gates, exchange=late_weights[1])
    wout, wup3, wdown = late_weights[2](late_a, late_b)
    mix, gdn_raw, states = _gdn_scan(u, w, qd, kd, a_intra, gl, proj, gdn_out_norm, mix)
    mixed = _matmul(mix, wout, tm=2048, tk=1024, name="mm_out")
    x1, h2 = _post_mix(x, mixed, post_mix_norm, pre_mlp_norm)

    def relu2(acc):
        r = jnp.maximum(acc, 0.0)
        return r, r * r

    up_relu, act = _matmul(h2, wup3, b3=True, tm=1024, tn=1024, tk=1024, out_dtypes=(BF16, BF16), epilogue=relu2,
                           name="mm_up")
    y = _matmul(act, wdown, tm=2048, tk=1024, name="mm_down")
    dx2, dy, d_post_mlp, loss_row = _loss_head(x1, y, post_mlp_norm, target)

    dwdown = _matmul(act, dy, ta=True, tm=1024, tn=1024, tk=2048, out_dtypes=(BF16,), name="mm_dwdown")

    def relu2_bwd(acc, r):
        return (acc * 2.0 * r.astype(F32),)

    dup = _matmul(dy, wdown, tb=True, tm=1024, tn=1024, tk=1024, out_dtypes=(BF16,), extra=(up_relu,), epilogue=relu2_bwd,
                  name="mm_dact")
    dwup3 = _matmul(h2, dup, ta=True, tm=1024, tn=1024, tk=2048, out_dtypes=(BF16,), o3=True, name="mm_dwup")
    dh2 = _matmul(dup, wup3, tb=True, b3=True, tm=2048, tk=1024, name="mm_dh2")
    dx1, dmixed, d_pre_mlp, d_post_mix = _mid_bwd(dh2, x1, pre_mlp_norm, dx2, mixed, post_mix_norm)
    dwout = _matmul(mix, dmixed, ta=True, tm=1024, tn=1024, tk=2048, out_dtypes=(BF16,), name="mm_dwout")
    dmix = _matmul(dmixed, wout, tb=True, tm=2048, tk=1024, name="mm_dmix")

    dfox, delta, d_fox_norm, from_sibling = _fox_norm_bwd(fox_o, dmix, w2, exchange=reduce_late[0](dwout, dwup3, dwdown))
    dproj, dcum_fox, reduced_late = _fox_bwd(proj, dfox, gates, lse, delta, exchange=reduce_late[1](from_sibling))
    dproj, du, dw, dqd, dkd, da, dgl, d_gdn_norm = _gdn_scan_bwd(dmix, gdn_raw, proj, gdn_out_norm, u, w, qd, kd,
                                                                 a_intra, gl, states, dproj)
    dqkv, dgates_gdn = _gdn_prep_bwd(qkv, gates, t_inv, du, dw, dqd, dkd, da, dgl)
    dproj, d_conv = _gdn_pre_bwd(proj, conv_w, dqkv, dproj)
    dproj, sums = _gates_bwd(proj, bias_vec, alog_vec, dgates_gdn, dcum_fox, dproj)

    dwin_p = _matmul(dproj, h, ta=True, tm=1280, tn=1024, tk=2048, out_dtypes=(BF16,), name="mm_dwin")
    exchange_in = reduce_in(dwin_p)
    dh = _matmul(dproj, win_p, tm=2048, tk=1280, name="mm_dh", exchange=exchange_in)
    dh, reduced_in = dh if exchange_in is not None else (dh, [])
    grad_x, d_pre_mix = _pre_norm_bwd(dh, x, pre_mix_norm, dx1)

    d_norms = jnp.concatenate([d_pre_mix, d_post_mix, d_pre_mlp, d_post_mlp], axis=0)
    return loss_row[0, 0], grad_x, (d_norms, d_conv, sums, d_fox_norm, d_gdn_norm), reduced_late, reduced_in


def kernel(x, pre_mix_norm, w_in, fox_f_bias, fox_out_norm, gdn_conv_w, gdn_a_log, gdn_dt_bias, gdn_out_norm, w_out, post_mix_norm, pre_mlp_norm, w_up, w_down, post_mlp_norm, loss_target, m_pre_mix_norm, m_w_in, m_fox_f_bias, m_fox_out_norm, m_gdn_conv_w, m_gdn_a_log, m_gdn_dt_bias, m_gdn_out_norm, m_w_out, m_post_mix_norm, m_pre_mlp_norm, m_w_up, m_w_down, m_post_mlp_norm, v_pre_mix_norm, v_w_in, v_fox_f_bias, v_fox_out_norm, v_gdn_conv_w, v_gdn_a_log, v_gdn_dt_bias, v_gdn_out_norm, v_w_out, v_post_mix_norm, v_pre_mlp_norm, v_w_up, v_w_down, v_post_mlp_norm):
    weights = dict(pre_mix_norm=pre_mix_norm, w_in=w_in, fox_f_bias=fox_f_bias, fox_out_norm=fox_out_norm, gdn_conv_w=gdn_conv_w,
                   gdn_a_log=gdn_a_log, gdn_dt_bias=gdn_dt_bias, gdn_out_norm=gdn_out_norm, w_out=w_out, post_mix_norm=post_mix_norm,
                   pre_mlp_norm=pre_mlp_norm, w_up=w_up, w_down=w_down, post_mlp_norm=post_mlp_norm)
    m_in = dict(pre_mix_norm=m_pre_mix_norm, w_in=m_w_in, fox_f_bias=m_fox_f_bias, fox_out_norm=m_fox_out_norm, gdn_conv_w=m_gdn_conv_w,
                gdn_a_log=m_gdn_a_log, gdn_dt_bias=m_gdn_dt_bias, gdn_out_norm=m_gdn_out_norm, w_out=m_w_out, post_mix_norm=m_post_mix_norm,
                pre_mlp_norm=m_pre_mlp_norm, w_up=m_w_up, w_down=m_w_down, post_mlp_norm=m_post_mlp_norm)
    v_in = dict(pre_mix_norm=v_pre_mix_norm, w_in=v_w_in, fox_f_bias=v_fox_f_bias, fox_out_norm=v_fox_out_norm, gdn_conv_w=v_gdn_conv_w,
                gdn_a_log=v_gdn_a_log, gdn_dt_bias=v_gdn_dt_bias, gdn_out_norm=v_gdn_out_norm, w_out=v_w_out, post_mix_norm=v_post_mix_norm,
                pre_mlp_norm=v_pre_mlp_norm, w_up=v_w_up, w_down=v_w_down, post_mlp_norm=v_post_mlp_norm)
    order_w = ("pre_mix_norm", "w_in", "fox_f_bias", "fox_out_norm", "gdn_conv_w", "gdn_a_log", "gdn_dt_bias", "gdn_out_norm", "w_out",
               "post_mix_norm", "pre_mlp_norm", "w_up", "w_down", "post_mlp_norm")
    big = ("w_in", "w_out", "w_up", "w_down")

    def row(v):
        return v if v.ndim == 2 else v.reshape(1, -1)

    win_shard = jnp.pad(w_in.T.astype(BF16), ((0, D - CW), (0, 0)))

    def resolve_first(gathered):
        win_g, conv_g = gathered
        return (_to_padded_rows(_with_own(win_g, win_shard)),
                _with_own(conv_g, gdn_conv_w).transpose(1, 0, 2).reshape(CONV_K, 3 * DGDN))

    late_shards = [weights[n].astype(BF16) for n in big[1:]]

    def resolve_late(gathered_a, gathered_b):
        wout_g, wup3, wdown_g = [_with_own(g, own) for g, own in zip(list(gathered_a) + list(gathered_b), late_shards)]
        return wout_g.reshape(D, D), wup3, wdown_g.reshape(DFF, D)

    pair_sums, late_blocks = {}, []

    def chip_exchange_of(names, blocks, theirs):
        for n, s in zip(names, _pair_sum(blocks, theirs, "grads_pair_sum_" + names[0])):
            pair_sums[n] = s
        return _chip_exchange([pair_sums[n] for n in names])

    def late_pair_exchange(dwout, dwup3, dwdown):
        late_blocks.extend([dwout.reshape(NCHIP, D // NCHIP, D), dwup3, dwdown.reshape(NCHIP, DFF // NCHIP, D)])
        return _pair_exchange(late_blocks)

    def late_chip_exchange(theirs):
        return chip_exchange_of(big[1:], late_blocks, theirs)

    def reduce_in(dwin_p):
        blocks = [_from_padded_rows(dwin_p)]
        return chip_exchange_of(big[:1], blocks, _run_exchange(_pair_exchange(blocks), "grads_pair_exchange_w_in"))

    loss_local, grad_x, small, received_late, received_in = _local_step(
        x[0], loss_target[0], (_allgather_exchange([win_shard], whole=[gdn_conv_w]), resolve_first),
        (_allgather_exchange(late_shards[:2]), _allgather_exchange(late_shards[2:]), resolve_late),
        (late_pair_exchange, late_chip_exchange), reduce_in, row(pre_mix_norm), fox_f_bias, row(fox_out_norm),
        gdn_a_log, gdn_dt_bias,
        row(gdn_out_norm), row(post_mix_norm), row(pre_mlp_norm), row(post_mlp_norm))
    loss = lax.psum(loss_local, ("x", "y", "c"))

    g_mine = _chip_sum([pair_sums[n] for n in big], list(received_in) + list(received_late))
    g_theirs = _run_exchange(_pair_share(g_mine), "grads_pair_share")

    g_big, d_big, nm_big, nv_big, small_gathered = _adamw_big(
        [weights[n] for n in big[1:]], g_mine[1:], g_theirs[1:], [m_in[n] for n in big[1:]], [v_in[n] for n in big[1:]],
        exchange=_small_gather(*small))
    in_t = _adamw_in(w_in.T, g_mine[0], g_theirs[0], m_w_in.T, v_w_in.T)
    g_small, d_small, nm_small, nv_small = _small_adamw(
        small_gathered, [row(weights[n]) for n in SMALL_NAMES], [row(m_in[n]) for n in SMALL_NAMES],
        [row(v_in[n]) for n in SMALL_NAMES])

    grads, delta, new_m, new_v = {}, {}, {}, {}
    grads["w_in"], delta["w_in"], new_m["w_in"], new_v["w_in"] = [t.T for t in in_t]
    for i, n in enumerate(big[1:]):
        grads[n], delta[n], new_m[n], new_v[n] = g_big[i], d_big[i], nm_big[i], nv_big[i]
    for i, n in enumerate(SMALL_NAMES):
        shape = weights[n].shape
        grads[n], delta[n], new_m[n], new_v[n] = (g_small[i].reshape(shape), d_small[i].reshape(shape),
                                                  nm_small[i].reshape(shape), nv_small[i].reshape(shape))
    return (loss, grad_x[None], *[grads[n] for n in order_w], *[delta[n] for n in order_w], *[new_m[n] for n in order_w],
            *[new_v[n] for n in order_w])
```

```python
import jax
import jax.numpy as jnp
from jax import lax
from jax.experimental import pallas as pl
from jax.experimental.pallas import tpu as pltpu

F32 = jnp.float32
BF16 = jnp.bfloat16
MESH = pl.DeviceIdType.MESH

S = 2048
D = 1024
NFH, FHD = 8, 64
NPAIR = NFH // 2
NGH, GHD = 4, 128
DFOX = NFH * FHD
DGDN = NGH * GHD
CHUNK = 64
NCH = S // CHUNK
CONV_K = 4
DFF = 4 * D
EPS = 1e-6
DPROJ = 3600
LANES = 128
DPROJ_PAD = 3840
BLK_GDN = 12
BLK_GZ = 24
BLK_SMALL = 28
NCHIP = 4
NDEV = 8
VMEM_LIMIT = 56 * 1024 * 1024

ADAM_LR = 0.001
ADAM_B1 = 0.9
ADAM_B2 = 0.999
ADAM_EPS = 1e-08
ADAM_WD = 0.01
ADAM_STEP = 10


def _cparams(**kw):
    return pltpu.CompilerParams(vmem_limit_bytes=VMEM_LIMIT, **kw)


def _dn(ca, cb):
    return (((ca,), (cb,)), ((), ()))


def _dot(a, b, ca=1, cb=0):
    return lax.dot_general(a.astype(BF16), b.astype(BF16), _dn(ca, cb), preferred_element_type=F32)


def _hdot(a, b, ca=1, cb=0):
    return lax.dot_general(a.astype(F32), b.astype(F32), _dn(ca, cb), precision=lax.Precision.HIGHEST,
                           preferred_element_type=F32)


def _dot3(a, b, ca=1, cb=0):
    a_hi, b_hi = a.astype(BF16), b.astype(BF16)
    a_lo, b_lo = (a - a_hi.astype(F32)).astype(BF16), (b - b_hi.astype(F32)).astype(BF16)
    dn = _dn(ca, cb)
    return (lax.dot_general(a_hi, b_hi, dn, preferred_element_type=F32)
            + (lax.dot_general(a_hi, b_lo, dn, preferred_element_type=F32)
               + lax.dot_general(a_lo, b_hi, dn, preferred_element_type=F32)))


@jax.custom_vjp
def _mm_nn(a, b):
    return _dot(a, b, 1, 0)


def _mm_nn_fwd(a, b):
    return _dot(a, b, 1, 0), (a, b)


def _mm_nn_bwd(res, g):
    a, b = res
    return _dot(g, b, 1, 1), _dot(a, g, 0, 0)


_mm_nn.defvjp(_mm_nn_fwd, _mm_nn_bwd)


@jax.custom_vjp
def _mm_nt(a, b):
    return _dot(a, b, 1, 1)


def _mm_nt_fwd(a, b):
    return _dot(a, b, 1, 1), (a, b)


def _mm_nt_bwd(res, g):
    a, b = res
    return _dot(g, b, 1, 0), _dot(g, a, 0, 0)


_mm_nt.defvjp(_mm_nt_fwd, _mm_nt_bwd)


@jax.custom_vjp
def _saved_inverse(m, t_inv):
    del m
    return t_inv


def _saved_inverse_fwd(m, t_inv):
    del m
    return t_inv, t_inv


def _saved_inverse_bwd(t_inv, g):
    return -_dot3(_dot3(t_inv, g, 0, 0), t_inv, 1, 1), jnp.zeros_like(t_inv)


_saved_inverse.defvjp(_saved_inverse_fwd, _saved_inverse_bwd)


def _sigmoid(z):
    return 1.0 / (1.0 + jnp.exp(-z))


def _softplus(z):
    return jnp.maximum(z, 0.0) + jnp.log(1.0 + jnp.exp(-jnp.abs(z)))


def _silu(z):
    return z * _sigmoid(z)


def _rms_scale(x):
    return lax.rsqrt(jnp.mean(x * x, axis=-1, keepdims=True) + EPS)


def _rms_bwd(x, w, g):
    r = _rms_scale(x)
    gw = g * w
    dx = r * gw - x * (r * r * r) * jnp.mean(gw * x, axis=-1, keepdims=True)
    return dx, g * x * r


def _matmul(a, b, *, name, ta=False, tb=False, tm=512, tn=512, tk=512, out_dtypes=(F32,), b3=False, o3=False,
            extra=(), epilogue=None, exchange=None):
    m, k = (a.shape[1], a.shape[0]) if ta else a.shape
    if b3:
        n = b.shape[1] if tb else b.shape[0] * b.shape[2]
        kb = b.shape[0] * b.shape[2] if tb else b.shape[1]
    else:
        n, kb = (b.shape[0], b.shape[1]) if tb else (b.shape[1], b.shape[0])
    assert kb == k, (name, kb, k)
    tm, tn, tk = min(tm, m), min(tn, n), min(tk, k)
    assert m % tm == 0 and n % tn == 0 and k % tk == 0, (name, m, n, k, tm, tn, tk)
    nk = k // tk
    n_extra = len(extra)
    n_out = len(out_dtypes)
    grid = (m // tm, n // tn, nk)
    ex_in, ex_in_specs, ex_out_specs, ex_out_shape, ex_scratch = _hosted(exchange)

    def body(*refs):
        a_ref, b_ref = refs[0], refs[1]
        extra_refs = refs[2:2 + n_extra]
        first_out = 2 + n_extra + len(ex_in)
        out_refs = refs[first_out:first_out + n_out]
        ex_refs = refs[2 + n_extra:first_out] + refs[first_out + n_out:first_out + n_out + len(ex_out_shape)] + refs[-2:]
        step = [pl.program_id(d) for d in range(3)]

        if exchange is not None:
            @pl.when((step[0] == 0) & (step[1] == 0) & (step[2] == 0))
            def _():
                exchange.start(*exchange.split(ex_refs))

        def finish(acc):
            outs = (acc,) if epilogue is None else epilogue(acc, *[r[...] for r in extra_refs])
            for o_ref, val in zip(out_refs, outs):
                o_ref[...] = val.astype(o_ref.dtype)

        part = _dot(a_ref[...], b_ref[...], 0 if ta else 1, 1 if tb else 0)
        if nk == 1:
            finish(part)
        else:
            acc_ref = refs[first_out + n_out + len(ex_out_shape)]

            @pl.when(step[2] == 0)
            def _():
                acc_ref[...] = part

            @pl.when(step[2] > 0)
            def _():
                acc_ref[...] += part

            @pl.when(step[2] == nk - 1)
            def _():
                finish(acc_ref[...])

        if exchange is not None:
            @pl.when((step[0] == grid[0] - 1) & (step[1] == grid[1] - 1) & (step[2] == nk - 1))
            def _():
                exchange.finish(*exchange.split(ex_refs))

    a_spec = pl.BlockSpec((tk, tm), lambda i, j, kk: (kk, i)) if ta else pl.BlockSpec((tm, tk), lambda i, j, kk: (i, kk))
    if b3 and tb:
        assert b.shape[2] == tk
        b_spec = pl.BlockSpec((None, tn, tk), lambda i, j, kk: (kk, j, 0))
    elif b3:
        assert b.shape[2] == tn
        b_spec = pl.BlockSpec((None, tk, tn), lambda i, j, kk: (j, kk, 0))
    elif tb:
        b_spec = pl.BlockSpec((tn, tk), lambda i, j, kk: (j, kk))
    else:
        b_spec = pl.BlockSpec((tk, tn), lambda i, j, kk: (kk, j))
    tile = pl.BlockSpec((tm, tn), lambda i, j, kk: (i, j))
    out_specs = [tile] * n_out
    out_shape = [jax.ShapeDtypeStruct((m, n), dt) for dt in out_dtypes]
    if o3:
        out_specs[0] = pl.BlockSpec((None, tm, tn), lambda i, j, kk: (j, i, 0))
        out_shape[0] = jax.ShapeDtypeStruct((n // tn, m, tn), out_dtypes[0])
    res = pl.pallas_call(
        body, name=name, grid=grid,
        in_specs=[a_spec, b_spec] + [tile] * n_extra + ex_in_specs, out_specs=out_specs + ex_out_specs,
        out_shape=out_shape + ex_out_shape,
        scratch_shapes=([pltpu.VMEM((tm, tn), F32)] if nk > 1 else []) + ex_scratch,
        compiler_params=_cparams(),
    )(a, b, *extra, *ex_in)
    if exchange is not None:
        return (res[0] if n_out == 1 else res[:n_out]), res[n_out:]
    return res[0] if n_out == 1 else res


TR = 256


def _row_spec(cols):
    return pl.BlockSpec((TR, cols), lambda i: (i, 0))


def _vec_spec(cols):
    return pl.BlockSpec((1, cols), lambda i: (0, 0))


def _pre_norm(x, w, exchange=None):
    ex_in, ex_in_specs, ex_out_specs, ex_out_shape, ex_scratch = _hosted(exchange)

    def body(*refs):
        x_ref, w_ref, h_ref = refs[0], refs[1], refs[2 + len(ex_in)]
        ex_refs = refs[2:2 + len(ex_in)] + refs[3 + len(ex_in):]
        if exchange is not None:
            @pl.when(pl.program_id(0) == 0)
            def _():
                exchange.start(*exchange.split(ex_refs))

        xv = x_ref[...]
        h_ref[...] = (xv * _rms_scale(xv) * w_ref[...]).astype(BF16)

        if exchange is not None:
            @pl.when(pl.program_id(0) == S // TR - 1)
            def _():
                exchange.finish(*exchange.split(ex_refs))

    res = pl.pallas_call(
        body, name="pre_norm", grid=(S // TR,), in_specs=[_row_spec(D), _vec_spec(D)] + ex_in_specs,
        out_specs=[_row_spec(D)] + ex_out_specs, out_shape=[jax.ShapeDtypeStruct((S, D), BF16)] + ex_out_shape,
        scratch_shapes=ex_scratch, compiler_params=_cparams(),
    )(x, w, *ex_in)
    return res[0], res[1:]


def _post_mix(x, mixed, w_post, w_pre_mlp):
    def body(x_ref, m_ref, wp_ref, wm_ref, x1_ref, h2_ref):
        mv = m_ref[...]
        x1 = x_ref[...] + mv * _rms_scale(mv) * wp_ref[...]
        x1_ref[...] = x1
        h2_ref[...] = (x1 * _rms_scale(x1) * wm_ref[...]).astype(BF16)

    return pl.pallas_call(
        body, name="post_mix", grid=(S // TR,),
        in_specs=[_row_spec(D), _row_spec(D), _vec_spec(D), _vec_spec(D)], out_specs=[_row_spec(D), _row_spec(D)],
        out_shape=[jax.ShapeDtypeStruct((S, D), F32), jax.ShapeDtypeStruct((S, D), BF16)], compiler_params=_cparams(),
    )(x, mixed, w_post, w_pre_mlp)


def _loss_head(x1, y, w_post_mlp, target):
    def body(x1_ref, y_ref, w_ref, t_ref, dx2_ref, dy_ref, dw_ref, loss_ref):
        i = pl.program_id(0)
        yv = y_ref[...]
        w = w_ref[...]
        x2 = x1_ref[...] + yv * _rms_scale(yv) * w
        err = x2 - t_ref[...]
        dx2 = err * (1.0 / D)
        dx2_ref[...] = dx2
        dy, dwt = _rms_bwd(yv, w, dx2)
        dy_ref[...] = dy.astype(BF16)

        @pl.when(i == 0)
        def _():
            dw_ref[...] = jnp.zeros_like(dw_ref)
            loss_ref[...] = jnp.zeros_like(loss_ref)

        dw_ref[...] += jnp.sum(dwt, axis=0, keepdims=True)
        part = 0.5 * jnp.sum(jnp.mean(err * err, axis=-1, keepdims=True), axis=0, keepdims=True)
        loss_ref[...] += jnp.broadcast_to(part, loss_ref.shape)

    return pl.pallas_call(
        body, name="loss_head", grid=(S // TR,),
        in_specs=[_row_spec(D), _row_spec(D), _vec_spec(D), _row_spec(D)],
        out_specs=[_row_spec(D), _row_spec(D), _vec_spec(D), _vec_spec(LANES)],
        out_shape=[jax.ShapeDtypeStruct((S, D), F32), jax.ShapeDtypeStruct((S, D), BF16),
                   jax.ShapeDtypeStruct((1, D), F32), jax.ShapeDtypeStruct((1, LANES), F32)],
        compiler_params=_cparams(),
    )(x1, y, w_post_mlp, target)


def _mid_bwd(dh2, x1, w_pre_mlp, dx2, mixed, w_post):
    def body(dh2_ref, x1_ref, wm_ref, dx2_ref, m_ref, wp_ref, dx1_ref, dm_ref, dwm_ref, dwp_ref):
        i = pl.program_id(0)
        dxa, dwm = _rms_bwd(x1_ref[...], wm_ref[...], dh2_ref[...])
        dx1 = dx2_ref[...] + dxa
        dx1_ref[...] = dx1
        dm, dwp = _rms_bwd(m_ref[...], wp_ref[...], dx1)
        dm_ref[...] = dm.astype(BF16)

        @pl.when(i == 0)
        def _():
            dwm_ref[...] = jnp.zeros_like(dwm_ref)
            dwp_ref[...] = jnp.zeros_like(dwp_ref)

        dwm_ref[...] += jnp.sum(dwm, axis=0, keepdims=True)
        dwp_ref[...] += jnp.sum(dwp, axis=0, keepdims=True)

    return pl.pallas_call(
        body, name="mid_bwd", grid=(S // TR,),
        in_specs=[_row_spec(D), _row_spec(D), _vec_spec(D), _row_spec(D), _row_spec(D), _vec_spec(D)],
        out_specs=[_row_spec(D), _row_spec(D), _vec_spec(D), _vec_spec(D)],
        out_shape=[jax.ShapeDtypeStruct((S, D), F32), jax.ShapeDtypeStruct((S, D), BF16),
                   jax.ShapeDtypeStruct((1, D), F32), jax.ShapeDtypeStruct((1, D), F32)],
        compiler_params=_cparams(),
    )(dh2, x1, w_pre_mlp, dx2, mixed, w_post)


def _pre_norm_bwd(dh, x, w, dx1):
    def body(dh_ref, x_ref, w_ref, dx1_ref, dx_ref, dw_ref):
        i = pl.program_id(0)
        dxa, dwt = _rms_bwd(x_ref[...], w_ref[...], dh_ref[...])
        dx_ref[...] = dx1_ref[...] + dxa

        @pl.when(i == 0)
        def _():
            dw_ref[...] = jnp.zeros_like(dw_ref)

        dw_ref[...] += jnp.sum(dwt, axis=0, keepdims=True)

    return pl.pallas_call(
        body, name="pre_norm_bwd", grid=(S // TR,),
        in_specs=[_row_spec(D), _row_spec(D), _vec_spec(D), _row_spec(D)], out_specs=[_row_spec(D), _vec_spec(D)],
        out_shape=[jax.ShapeDtypeStruct((S, D), F32), jax.ShapeDtypeStruct((1, D), F32)], compiler_params=_cparams(),
    )(dh, x, w, dx1)


BQ = 256
NQ = S // BQ
LANE_BETA, LANE_G = 8, 12


def _gate_lanes(shape):
    lane = lax.broadcasted_iota(jnp.int32, shape, 1)
    return lane < LANE_BETA, (lane >= LANE_BETA) & (lane < LANE_G), (lane >= LANE_G) & (lane < LANE_G + NGH)


def _gates(proj, bias_vec, alog_vec):
    def body(s_ref, b_ref, a_ref, o_ref, carry_ref):
        i = pl.program_id(0)

        @pl.when(i == 0)
        def _():
            carry_ref[...] = jnp.zeros_like(carry_ref)

        z = s_ref[...] + b_ref[...]
        tail = jnp.log(1.0 + jnp.exp(-jnp.abs(z)))
        sp = jnp.maximum(z, 0.0) + tail
        lf = jnp.minimum(z, 0.0) - tail
        r = lax.broadcasted_iota(jnp.int32, (BQ, BQ), 0)
        c = lax.broadcasted_iota(jnp.int32, (BQ, BQ), 1)
        tri = (c <= r).astype(F32)
        cum = _hdot(tri, lf) + carry_ref[...]
        carry_ref[...] = cum[BQ - 1:BQ, :]
        is_fox, is_beta, is_g = _gate_lanes(z.shape)
        o_ref[...] = jnp.where(is_fox, cum, jnp.where(is_beta, _sigmoid(z), jnp.where(is_g, -jnp.exp(a_ref[...]) * sp, 0.0)))

    return pl.pallas_call(
        body, name="gates", grid=(NQ,),
        in_specs=[pl.BlockSpec((BQ, LANES), lambda i: (i, BLK_SMALL)), _vec_spec(LANES), _vec_spec(LANES)],
        out_specs=pl.BlockSpec((BQ, LANES), lambda i: (i, 0)), out_shape=jax.ShapeDtypeStruct((S, LANES), F32),
        scratch_shapes=[pltpu.VMEM((1, LANES), F32)], compiler_params=_cparams(),
    )(proj, bias_vec, alog_vec)


def _gates_bwd(proj, bias_vec, alog_vec, dgates_gdn, dcum_fox, dproj):
    def body(s_ref, b_ref, a_ref, dg_ref, dc_ref, dproj_in, dproj_ref, red_ref, carry_ref):
        del dproj_in
        i = pl.program_id(0)

        @pl.when(i == 0)
        def _():
            carry_ref[...] = jnp.zeros_like(carry_ref)
            red_ref[...] = jnp.zeros_like(red_ref)

        z = s_ref[...] + b_ref[...]
        dg = dg_ref[...] + dc_ref[...]
        r = lax.broadcasted_iota(jnp.int32, (BQ, BQ), 0)
        c = lax.broadcasted_iota(jnp.int32, (BQ, BQ), 1)
        upper = (c >= r).astype(F32)
        dlf = _hdot(upper, dg) + carry_ref[...]
        carry_ref[...] = dlf[0:1, :]
        sig = _sigmoid(z)
        g_scale = -jnp.exp(a_ref[...])
        is_fox, is_beta, is_g = _gate_lanes(z.shape)
        ds = jnp.where(is_fox, dlf * (1.0 - sig), jnp.where(is_beta, dg * sig * (1.0 - sig), jnp.where(is_g, dg * g_scale * sig, 0.0)))
        dproj_ref[:, 0:LANES] = ds.astype(BF16)
        dproj_ref[:, LANES:2 * LANES] = jnp.zeros((BQ, LANES), BF16)
        dalog = jnp.where(is_g, dg * g_scale * _softplus(z), 0.0)
        sums = jnp.sum(ds, axis=0, keepdims=True)
        red_ref[0:1, :] += jnp.where(is_fox[0:1], sums, 0.0)
        red_ref[1:2, :] += pltpu.roll(jnp.where(is_g[0:1], sums, 0.0), LANES - LANE_G, 1)
        red_ref[2:3, :] += pltpu.roll(jnp.sum(dalog, axis=0, keepdims=True), LANES - LANE_G, 1)

    blk = pl.BlockSpec((BQ, LANES), lambda i: (NQ - 1 - i, 0))
    return pl.pallas_call(
        body, name="gates_bwd", grid=(NQ,),
        in_specs=[pl.BlockSpec((BQ, LANES), lambda i: (NQ - 1 - i, BLK_SMALL)), _vec_spec(LANES), _vec_spec(LANES), blk, blk,
                  pl.BlockSpec(memory_space=pl.ANY)],
        out_specs=[pl.BlockSpec((BQ, 2 * LANES), lambda i: (NQ - 1 - i, BLK_SMALL // 2)), pl.BlockSpec((8, LANES), lambda i: (0, 0))],
        out_shape=[jax.ShapeDtypeStruct((S, DPROJ_PAD), BF16), jax.ShapeDtypeStruct((8, LANES), F32)],
        input_output_aliases={5: 0},
        scratch_shapes=[pltpu.VMEM((1, LANES), F32)], compiler_params=_cparams(),
    )(proj, bias_vec, alog_vec, dgates_gdn, dcum_fox, dproj)


FOX_SCALE = FHD ** -0.5
FOX_PAIRS = 2
FOX_PAIRS_BWD = 2


def _head_mask(e):
    lane = lax.broadcasted_iota(jnp.int32, (1, LANES), 1)
    return (lane >= e * FHD) & (lane < (e + 1) * FHD)


def _lane_col(vals, index):
    lane = lax.broadcasted_iota(jnp.int32, vals.shape, 1)
    return jnp.sum(jnp.where(lane == index, vals, 0.0), axis=1, keepdims=True)


def _sublane_row(vals, index):
    row = lax.broadcasted_iota(jnp.int32, vals.shape, 0)
    return jnp.sum(jnp.where(row == index, vals, 0.0), axis=0, keepdims=True)


def _pair_cols(c0, c1):
    lane = lax.broadcasted_iota(jnp.int32, (c0.shape[0], 2), 1)
    return jnp.where(lane == 0, c0, c1)


def _split3(x):
    hi = x.astype(BF16).astype(F32)
    rest = x - hi
    mid = rest.astype(BF16).astype(F32)
    return hi, mid, (rest - mid).astype(BF16).astype(F32)


def _fox_operand(vals, e, cum, is_query):
    lane = lax.broadcasted_iota(jnp.int32, (1, LANES), 1)
    base = (1 - e) * FHD
    parts = _split3(cum)
    own = jnp.where(_head_mask(e), vals * FOX_SCALE if is_query else vals, 0.0)
    cum_at, ones_at = (base, base + 3) if is_query else (base + 3, base)
    sign = 1.0 if is_query else -1.0
    out = own + jnp.where((lane >= ones_at) & (lane < ones_at + 3), 1.0, 0.0)
    for i, part in enumerate(parts):
        out = out + jnp.where(lane == cum_at + i, sign * part, 0.0)
    return out.astype(BF16)


def _causal_block():
    return lax.broadcasted_iota(jnp.int32, (BQ, BQ), 1) <= lax.broadcasted_iota(jnp.int32, (BQ, BQ), 0)


def _head_rms(o, masks):
    o2 = o * o
    r = [lax.rsqrt(jnp.sum(jnp.where(mk, o2, 0.0), axis=1, keepdims=True) * (1.0 / FHD) + EPS) for mk in masks]
    return jnp.where(masks[0], r[0], r[1])


def _hosted(exchange):
    if exchange is None:
        return [], [], [], [], []
    return (exchange.inputs, [HBM] * len(exchange.inputs), [HBM] * len(exchange.out_shape), exchange.out_shape,
            exchange.sem_shapes())


def _fox_fwd(proj, gates, w2, exchange=None):
    ex_in, ex_in_specs, ex_out_specs, ex_out_shape, ex_scratch = _hosted(exchange)

    n_in = 3 * FOX_PAIRS + 2
    heads = [(pp, e) for pp in range(FOX_PAIRS) for e in range(2)]

    def body(*refs):
        qkv_refs, g_ref, w_ref = refs[:3 * FOX_PAIRS], refs[3 * FOX_PAIRS], refs[3 * FOX_PAIRS + 1]
        mix_ref, o_ref, lse_ref = refs[n_in + len(ex_in):n_in + 3 + len(ex_in)]
        ka_ref, vb_ref = refs[n_in + 3 + len(ex_in) + len(ex_out_shape):n_in + 5 + len(ex_in) + len(ex_out_shape)]
        ex_refs = refs[n_in:n_in + len(ex_in)] + refs[n_in + 3 + len(ex_in):n_in + 3 + len(ex_in) + len(ex_out_shape)] + refs[-2:]
        grp, qi = pl.program_id(0), pl.program_id(1)

        def head_index(pp, e):
            return 2 * (FOX_PAIRS * grp + pp) + e

        if exchange is not None:
            @pl.when((grp == 0) & (qi == 0))
            def _():
                exchange.start(*exchange.split(ex_refs))

        @pl.when(qi == 0)
        def _():
            gt = g_ref[...]
            for pp in range(FOX_PAIRS):
                kv = qkv_refs[3 * pp + 1][...]
                for e in range(2):
                    ka_ref[2 * pp + e] = _fox_operand(kv, e, _lane_col(gt, head_index(pp, e)), False)
                vb_ref[pp] = qkv_refs[3 * pp + 2][...].astype(BF16)

        masks = [_head_mask(0), _head_mask(1)]
        gt = g_ref[pl.ds(pl.multiple_of(qi * BQ, BQ), BQ), :]
        qs = [_fox_operand(qkv_refs[3 * pp][...], e, _lane_col(gt, head_index(pp, e)), True) for pp, e in heads]
        n = range(len(heads))

        def block(kj, carry, diagonal):
            rows = pl.ds(pl.multiple_of(kj * BQ, BQ), BQ)
            s = [_dot(qs[i], ka_ref[i, rows, :], 1, 1) for i in n]
            if diagonal:
                s = [jnp.where(_causal_block(), s[i], -jnp.inf) for i in n]
            m_new = [jnp.maximum(carry[i][0], jnp.max(s[i], axis=-1, keepdims=True)) for i in n]
            p = [jnp.exp(s[i] - m_new[i]) for i in n]
            alpha = [jnp.exp(carry[i][0] - m_new[i]) for i in n]
            l_new = [alpha[i] * carry[i][1] + jnp.sum(p[i], axis=-1, keepdims=True) for i in n]
            pv = [_dot(p[i], vb_ref[heads[i][0], rows, :]) for i in n]
            return tuple((m_new[i], l_new[i], alpha[i] * carry[i][2] + pv[i]) for i in n)

        one = (jnp.full((BQ, 1), -jnp.inf, F32), jnp.zeros((BQ, 1), F32), jnp.zeros((BQ, LANES), F32))
        below = lax.fori_loop(0, qi, lambda kj, carry: block(kj, carry, False), (one,) * len(heads))
        done = block(qi, below, True)
        for pp in range(FOX_PAIRS):
            (m0, l0, a0), (m1, l1, a1) = done[2 * pp], done[2 * pp + 1]
            o = jnp.where(masks[0], a0 / l0, a1 / l1)
            cols = slice(pp * LANES, (pp + 1) * LANES)
            o_ref[:, cols] = o
            mix_ref[:, cols] = (o * _head_rms(o, masks) * w_ref[...]).astype(BF16)
            lse_ref[pp] = _pair_cols(m0 + jnp.log(l0), m1 + jnp.log(l1))

        if exchange is not None:
            @pl.when((grp == NPAIR // FOX_PAIRS - 1) & (qi == NQ - 1))
            def _():
                exchange.finish(*exchange.split(ex_refs))

    qkv_specs = []
    for pp in range(FOX_PAIRS):
        qkv_specs.append(pl.BlockSpec((BQ, LANES), lambda g, i, pp=pp: (i, 3 * (FOX_PAIRS * g + pp))))
        qkv_specs.append(pl.BlockSpec((S, LANES), lambda g, i, pp=pp: (0, 3 * (FOX_PAIRS * g + pp) + 1)))
        qkv_specs.append(pl.BlockSpec((S, LANES), lambda g, i, pp=pp: (0, 3 * (FOX_PAIRS * g + pp) + 2)))
    blk = pl.BlockSpec((BQ, FOX_PAIRS * LANES), lambda g, i: (i, g))
    res = pl.pallas_call(
        body, name="fox_fwd", grid=(NPAIR // FOX_PAIRS, NQ),
        in_specs=qkv_specs + [pl.BlockSpec((S, LANES), lambda g, i: (0, 0)), pl.BlockSpec((1, LANES), lambda g, i: (0, 0))]
        + ex_in_specs,
        out_specs=[blk, blk, pl.BlockSpec((FOX_PAIRS, BQ, 2), lambda g, i: (g, i, 0))] + ex_out_specs,
        out_shape=[jax.ShapeDtypeStruct((S, D), BF16), jax.ShapeDtypeStruct((S, DFOX), F32),
                   jax.ShapeDtypeStruct((NPAIR, S, 2), F32)] + ex_out_shape,
        scratch_shapes=[pltpu.VMEM((2 * FOX_PAIRS, S, LANES), BF16), pltpu.VMEM((FOX_PAIRS, S, LANES), BF16)] + ex_scratch,
        compiler_params=_cparams(),
    )(*([proj] * (3 * FOX_PAIRS)), gates, w2, *ex_in)
    return res[0], res[1], res[2], res[3:]


def _fox_norm_bwd(o, dmix, w2, exchange=None):
    ex_in, ex_in_specs, ex_out_specs, ex_out_shape, ex_scratch = _hosted(exchange)

    def body(*refs):
        o_ref, g_ref, w_ref = refs[:3]
        do_ref, dl_ref, dw_ref = refs[3 + len(ex_in):6 + len(ex_in)]
        ex_refs = refs[3:3 + len(ex_in)] + refs[6 + len(ex_in):]
        hp, qi = pl.program_id(0), pl.program_id(1)

        if exchange is not None:
            @pl.when((hp == 0) & (qi == 0))
            def _():
                exchange.start(*exchange.split(ex_refs))

        masks = [_head_mask(0), _head_mask(1)]
        ov = o_ref[...]
        g = g_ref[...]
        r = _head_rms(ov, masks)
        gw = g * w_ref[...]
        gwo = gw * ov
        mean = [jnp.sum(jnp.where(mk, gwo, 0.0), axis=1, keepdims=True) * (1.0 / FHD) for mk in masks]
        do = r * gw - ov * (r * r * r) * jnp.where(masks[0], mean[0], mean[1])
        do_ref[...] = do.astype(BF16)
        doo = do * ov
        dl_ref[...] = _pair_cols(*[jnp.sum(jnp.where(mk, doo, 0.0), axis=1, keepdims=True) for mk in masks])

        @pl.when((hp == 0) & (qi == 0))
        def _():
            dw_ref[...] = jnp.zeros_like(dw_ref)

        dw_ref[...] += jnp.sum(g * ov * r, axis=0, keepdims=True)

        @pl.when((hp == NPAIR - 1) & (qi == NQ - 1))
        def _():
            dw = dw_ref[...]
            dw_ref[...] = dw + pltpu.roll(dw, FHD, 1)
            if exchange is not None:
                exchange.finish(*exchange.split(ex_refs))

    blk = pl.BlockSpec((BQ, LANES), lambda hp, i: (i, hp))
    vec = pl.BlockSpec((1, LANES), lambda hp, i: (0, 0))
    res = pl.pallas_call(
        body, name="fox_norm_bwd", grid=(NPAIR, NQ), in_specs=[blk, blk, vec] + ex_in_specs,
        out_specs=[blk, pl.BlockSpec((None, BQ, 2), lambda hp, i: (hp, i, 0)), vec] + ex_out_specs,
        out_shape=[jax.ShapeDtypeStruct((S, DFOX), BF16), jax.ShapeDtypeStruct((NPAIR, S, 2), F32),
                   jax.ShapeDtypeStruct((1, LANES), F32)] + ex_out_shape,
        scratch_shapes=ex_scratch, compiler_params=_cparams(),
    )(o, dmix, w2, *ex_in)
    return res[0], res[1], res[2], res[3:]


def _fox_bwd(proj, do, gates, lse, delta, exchange=None):
    ex_in, ex_in_specs, ex_out_specs, ex_out_shape, ex_scratch = _hosted(exchange)

    pg = FOX_PAIRS_BWD
    n_in = 3 * pg + 4
    heads = [(pp, e) for pp in range(pg) for e in range(2)]

    def body(*refs):
        qkv_refs = refs[:3 * pg]
        do_ref, g_ref, lse_ref, dl_ref = refs[3 * pg:n_in]
        dproj_ref, dc_ref = refs[n_in + len(ex_in):n_in + 2 + len(ex_in)]
        qa_ref, dq_ref = refs[n_in + 2 + len(ex_in) + len(ex_out_shape):n_in + 4 + len(ex_in) + len(ex_out_shape)]
        ex_refs = refs[n_in:n_in + len(ex_in)] + refs[n_in + 2 + len(ex_in):n_in + 2 + len(ex_in) + len(ex_out_shape)] + refs[-2:]
        grp, kj = pl.program_id(0), pl.program_id(1)

        def head_index(pp, e):
            return 2 * (pg * grp + pp) + e

        if exchange is not None:
            @pl.when((grp == 0) & (kj == 0))
            def _():
                exchange.start(*exchange.split(ex_refs))

        @pl.when(kj == 0)
        def _():
            gt = g_ref[...]
            for pp in range(pg):
                qv = qkv_refs[3 * pp][...]
                for e in range(2):
                    qa_ref[2 * pp + e] = _fox_operand(qv, e, _lane_col(gt, head_index(pp, e)), True)
            dq_ref[...] = jnp.zeros_like(dq_ref)

        @pl.when((grp == 0) & (kj == 0))
        def _():
            dc_ref[...] = jnp.zeros_like(dc_ref)

        masks = [_head_mask(0), _head_mask(1)]
        krows = pl.ds(pl.multiple_of(kj * BQ, BQ), BQ)
        gk = g_ref[krows, :]
        kas = [_fox_operand(qkv_refs[3 * pp + 1][...], e, _lane_col(gk, head_index(pp, e)), False) for pp, e in heads]
        vbs = [qkv_refs[3 * pp + 2][...].astype(BF16) for pp in range(pg)]
        lane = lax.broadcasted_iota(jnp.int32, (BQ, LANES), 1)
        n = range(len(heads))

        def block(qi, carry, diagonal):
            dks, dvs, css = carry
            rows = pl.ds(pl.multiple_of(qi * BQ, BQ), BQ)
            qa = [qa_ref[i, rows, :] for i in n]
            s = [_dot(qa[i], kas[i], 1, 1) for i in n]
            if diagonal:
                s = [jnp.where(_causal_block(), s[i], -jnp.inf) for i in n]
            dov = [do_ref[rows, pp * LANES:(pp + 1) * LANES] for pp in range(pg)]
            doe = [jnp.where(masks[e], dov[pp], jnp.zeros_like(dov[pp])) for pp, e in heads]
            lse2 = [lse_ref[pp, rows, :] for pp in range(pg)]
            dl2 = [dl_ref[pp, rows, :] for pp in range(pg)]
            p = [jnp.exp(s[i] - _lane_col(lse2[heads[i][0]], heads[i][1])) for i in n]
            dp = [_dot(doe[i], vbs[heads[i][0]], 1, 1) for i in n]
            ds = [p[i] * (dp[i] - _lane_col(dl2[heads[i][0]], heads[i][1])) for i in n]
            dv_part = [_dot(p[i], doe[i], 0, 0) for i in n]
            dk_part = [_dot(ds[i], jnp.where(masks[heads[i][1]], qa[i], jnp.zeros_like(qa[i])), 0, 0) for i in n]
            dq_part = [jnp.where(masks[heads[i][1]], _dot(ds[i], kas[i]), 0.0) for i in n]
            css = tuple(css[i] + jnp.sum(ds[i], axis=0, keepdims=True) for i in n)
            dc = jnp.zeros((BQ, LANES), F32)
            for i in n:
                dc = dc + jnp.where(lane == head_index(*heads[i]), jnp.sum(ds[i], axis=1, keepdims=True), 0.0)
            for pp in range(pg):
                dq_ref[pp, rows, :] += (dq_part[2 * pp] + dq_part[2 * pp + 1]) * FOX_SCALE
            dc_ref[rows, :] += dc
            dks = tuple(dks[pp] + dk_part[2 * pp] + dk_part[2 * pp + 1] for pp in range(pg))
            dvs = tuple(dvs[pp] + dv_part[2 * pp] + dv_part[2 * pp + 1] for pp in range(pg))
            return dks, dvs, css

        zero = jnp.zeros((BQ, LANES), F32)
        first = block(kj, ((zero,) * pg, (zero,) * pg, (jnp.zeros((1, BQ), F32),) * len(heads)), True)
        dks, dvs, css = lax.fori_loop(kj + 1, NQ, lambda qi, carry: block(qi, carry, False), first)
        r = lax.broadcasted_iota(jnp.int32, (BQ, BQ), 0)
        c = lax.broadcasted_iota(jnp.int32, (BQ, BQ), 1)
        dcol = jnp.zeros((BQ, LANES), F32)
        for i in n:
            col = jnp.sum(jnp.where(r == c, css[i], 0.0), axis=1, keepdims=True)
            dcol = dcol + jnp.where(lane == head_index(*heads[i]), col, 0.0)
        dc_ref[krows, :] -= dcol
        for pp in range(pg):
            base = 3 * pp * LANES
            dproj_ref[krows, base + LANES:base + 2 * LANES] = dks[pp].astype(BF16)
            dproj_ref[krows, base + 2 * LANES:base + 3 * LANES] = dvs[pp].astype(BF16)

        @pl.when(kj == NQ - 1)
        def _():
            for pp in range(pg):
                dproj_ref[:, 3 * pp * LANES:(3 * pp + 1) * LANES] = dq_ref[pp].astype(BF16)

        if exchange is not None:
            @pl.when((grp == NPAIR // pg - 1) & (kj == NQ - 1))
            def _():
                exchange.finish(*exchange.split(ex_refs))

    qkv_specs = []
    for pp in range(pg):
        qkv_specs.append(pl.BlockSpec((S, LANES), lambda g, j, pp=pp: (0, 3 * (pg * g + pp))))
        qkv_specs.append(pl.BlockSpec((BQ, LANES), lambda g, j, pp=pp: (j, 3 * (pg * g + pp) + 1)))
        qkv_specs.append(pl.BlockSpec((BQ, LANES), lambda g, j, pp=pp: (j, 3 * (pg * g + pp) + 2)))
    pair = pl.BlockSpec((pg, S, 2), lambda g, j: (g, 0, 0))
    res = pl.pallas_call(
        body, name="fox_bwd", grid=(NPAIR // pg, NQ),
        in_specs=qkv_specs + [pl.BlockSpec((S, pg * LANES), lambda g, j: (0, g)), pl.BlockSpec((S, LANES), lambda g, j: (0, 0)),
                              pair, pair] + ex_in_specs,
        out_specs=[pl.BlockSpec((S, 3 * pg * LANES), lambda g, j: (0, g)), pl.BlockSpec((S, LANES), lambda g, j: (0, 0))]
        + ex_out_specs,
        out_shape=[jax.ShapeDtypeStruct((S, DPROJ_PAD), BF16), jax.ShapeDtypeStruct((S, LANES), F32)] + ex_out_shape,
        scratch_shapes=[pltpu.VMEM((2 * pg, S, LANES), BF16), pltpu.VMEM((pg, S, LANES), F32)] + ex_scratch,
        compiler_params=_cparams(),
    )(*([proj] * (3 * pg)), do, gates, lse, delta, *ex_in)
    return res[0], res[1], res[2:]


NQKV = 3 * NGH
GDN_QSCALE = GHD ** -0.5


def _shift_down(x, s):
    if s == 0:
        return x
    row = lax.broadcasted_iota(jnp.int32, x.shape, 0)
    return jnp.where(row >= s, pltpu.roll(x, s, 0), 0.0)


def _shift_up(x, s):
    if s == 0:
        return x
    n = x.shape[0]
    row = lax.broadcasted_iota(jnp.int32, x.shape, 0)
    return jnp.where(row < n - s, pltpu.roll(x, n - s, 0), 0.0)


def _conv_pre(xv, wv):
    pre = xv * wv[CONV_K - 1:CONV_K, :]
    for j in range(CONV_K - 1):
        pre = pre + _shift_down(xv, CONV_K - 1 - j) * wv[j:j + 1, :]
    return pre


def _l2_factors(b):
    return b < 2 * NGH, jnp.where(b < NGH, GDN_QSCALE, 1.0)


def _gdn_pre(proj, conv_w):
    def body(x_ref, w_ref, o_ref):
        b = pl.program_id(0)
        c = _silu(_conv_pre(x_ref[...], w_ref[...]))
        normed, scale = _l2_factors(b)
        rs = lax.rsqrt(jnp.sum(c * c, axis=-1, keepdims=True) + EPS)
        o_ref[...] = c * jnp.where(normed, rs, 1.0) * scale

    return pl.pallas_call(
        body, name="gdn_pre", grid=(NQKV,),
        in_specs=[pl.BlockSpec((S, GHD), lambda b: (0, BLK_GDN + b)), pl.BlockSpec((CONV_K, GHD), lambda b: (0, b))],
        out_specs=pl.BlockSpec((S, GHD), lambda b: (0, b)),
        out_shape=jax.ShapeDtypeStruct((S, NQKV * GHD), F32), compiler_params=_cparams(),
    )(proj, conv_w)


def _gdn_pre_bwd(proj, conv_w, dqkv, dproj):
    def body(x_ref, w_ref, dy_ref, dproj_in, dx_ref, dw_ref):
        del dproj_in
        b = pl.program_id(0)
        xv = x_ref[...]
        wv = w_ref[...]
        pre = _conv_pre(xv, wv)
        sig = _sigmoid(pre)
        c = pre * sig
        normed, scale = _l2_factors(b)
        g = dy_ref[...] * scale
        rs = lax.rsqrt(jnp.sum(c * c, axis=-1, keepdims=True) + EPS)
        dc_n = rs * g - c * (rs * rs * rs) * jnp.sum(g * c, axis=-1, keepdims=True)
        dc = jnp.where(normed, dc_n, g)
        dpre = dc * sig * (1.0 + pre * (1.0 - sig))
        dx = dpre * wv[CONV_K - 1:CONV_K, :]
        for j in range(CONV_K - 1):
            dx = dx + _shift_up(dpre, CONV_K - 1 - j) * wv[j:j + 1, :]
        dx_ref[...] = dx.astype(BF16)
        for j in range(CONV_K):
            dw_ref[j:j + 1, :] = jnp.sum(dpre * _shift_down(xv, CONV_K - 1 - j), axis=0, keepdims=True)

    return pl.pallas_call(
        body, name="gdn_pre_bwd", grid=(NQKV,),
        in_specs=[pl.BlockSpec((S, GHD), lambda b: (0, BLK_GDN + b)), pl.BlockSpec((CONV_K, GHD), lambda b: (0, b)),
                  pl.BlockSpec((None, S, GHD), lambda b: (b // NGH, 0, b % NGH)), pl.BlockSpec(memory_space=pl.ANY)],
        out_specs=[pl.BlockSpec((S, GHD), lambda b: (0, BLK_GDN + b)), pl.BlockSpec((CONV_K, GHD), lambda b: (0, b))],
        out_shape=[jax.ShapeDtypeStruct((S, DPROJ_PAD), BF16), jax.ShapeDtypeStruct((CONV_K, NQKV * GHD), F32)],
        input_output_aliases={3: 0}, compiler_params=_cparams(),
    )(proj, conv_w, dqkv, dproj)


CB = 4
NCB = NCH // CB


def _chunk_prep(qs, ks, vs, gcols, bcols, t_saved=None):
    n = range(len(qs))
    r = lax.broadcasted_iota(jnp.int32, (CHUNK, CHUNK), 0)
    c = lax.broadcasted_iota(jnp.int32, (CHUNK, CHUNK), 1)
    incl = c <= r
    eye = (r == c).astype(F32)
    grow = [jnp.sum(gcols[i] * eye, axis=0, keepdims=True) for i in n]
    gc_col = [jnp.sum(jnp.where(incl, grow[i], 0.0), axis=1, keepdims=True) for i in n]
    gc_row = [jnp.sum(jnp.where(r <= c, gcols[i], 0.0), axis=0, keepdims=True) for i in n]
    decay = [jnp.exp(jnp.where(incl, gc_col[i] - gc_row[i], -jnp.inf)) for i in n]
    kb = [ks[i] * bcols[i] for i in n]
    vb = [vs[i] * bcols[i] for i in n]
    kk = [_mm_nt(kb[i], ks[i]) for i in n]
    m = [jnp.where(c < r, kk[i] * decay[i], 0.0) for i in n]
    if t_saved is None:
        t_inv = [eye - m[i] for i in n]
        p = [_dot3(m[i], m[i]) for i in n]
        for step in range(5):
            t_inv = [t_inv[i] + _dot3(t_inv[i], p[i]) for i in n]
            if step < 4:
                p = [_dot3(p[i], p[i]) for i in n]
    else:
        t_inv = [_saved_inverse(m[i], t_saved[i]) for i in n]
    egc = [jnp.exp(gc_col[i]) for i in n]
    u = [_mm_nn(t_inv[i], vb[i]) for i in n]
    w = [_mm_nn(t_inv[i], kb[i] * egc[i]) for i in n]
    qk = [_mm_nt(qs[i], ks[i]) for i in n]
    gc_last = [gc_col[i][CHUNK - 1:CHUNK, :] for i in n]
    return [(u[i], w[i], qk[i] * decay[i], qs[i] * egc[i], ks[i] * jnp.exp(gc_last[i] - gc_col[i]), jnp.exp(gc_last[i]),
             t_inv[i]) for i in n]


def _prep_specs():
    rows = CB * CHUNK
    qs = pl.BlockSpec((rows, GHD), lambda i, h: (i, h))
    ks = pl.BlockSpec((rows, GHD), lambda i, h: (i, NGH + h))
    vs = pl.BlockSpec((rows, GHD), lambda i, h: (i, 2 * NGH + h))
    gs = pl.BlockSpec((rows, LANES), lambda i, h: (i, 0))
    a_s = pl.BlockSpec((None, rows, CHUNK), lambda i, h: (h, i, 0))
    gl_s = pl.BlockSpec((None, CB, 1, LANES), lambda i, h: (h, i, 0, 0))
    return qs, ks, vs, gs, a_s, gl_s


def _gdn_prep(qkv, gates, exchange=None):
    ex_in, ex_in_specs, ex_out_specs, ex_out_shape, ex_scratch = _hosted(exchange)

    def body(*refs):
        q_ref, k_ref, v_ref, g_ref = refs[:4]
        u_ref, w_ref, qd_ref, kd_ref, a_ref, gl_ref, t_ref = refs[4 + len(ex_in):11 + len(ex_in)]
        ex_refs = refs[4:4 + len(ex_in)] + refs[11 + len(ex_in):]
        h = pl.program_id(1)

        if exchange is not None:
            @pl.when((pl.program_id(0) == 0) & (h == 0))
            def _():
                exchange.start(*exchange.split(ex_refs))

        chunks = [pl.ds(cidx * CHUNK, CHUNK) for cidx in range(CB)]
        gts = [g_ref[rows, :] for rows in chunks]
        outs = _chunk_prep([q_ref[rows, :] for rows in chunks], [k_ref[rows, :] for rows in chunks],
                           [v_ref[rows, :] for rows in chunks], [_lane_col(gt, LANE_G + h) for gt in gts],
                           [_lane_col(gt, LANE_BETA + h) for gt in gts])
        for cidx, rows in enumerate(chunks):
            u, w, a, qd, kd, gl, t_inv = outs[cidx]
            u_ref[rows, :] = u
            w_ref[rows, :] = w
            qd_ref[rows, :] = qd
            kd_ref[rows, :] = kd
            a_ref[rows, :] = a
            t_ref[rows, :] = t_inv
            gl_ref[cidx] = jnp.broadcast_to(gl, (1, LANES))

        if exchange is not None:
            @pl.when((pl.program_id(0) == NCB - 1) & (h == NGH - 1))
            def _():
                exchange.finish(*exchange.split(ex_refs))

    qs, ks, vs, gs, a_s, gl_s = _prep_specs()
    tok = jax.ShapeDtypeStruct((S, DGDN), F32)
    sq = jax.ShapeDtypeStruct((NGH, S, CHUNK), F32)
    res = pl.pallas_call(
        body, name="gdn_prep", grid=(NCB, NGH), in_specs=[qs, ks, vs, gs] + ex_in_specs,
        out_specs=[qs, qs, qs, qs, a_s, gl_s, a_s] + ex_out_specs,
        out_shape=[tok, tok, tok, tok, sq, jax.ShapeDtypeStruct((NGH, NCH, 1, LANES), F32), sq] + ex_out_shape,
        scratch_shapes=ex_scratch, compiler_params=_cparams(),
    )(qkv, qkv, qkv, gates, *ex_in)
    return res[:7], res[7:]


def _gdn_prep_bwd(qkv, gates, t_inv, du, dw, dqd, dkd, da, dgl):
    def body(q_ref, k_ref, v_ref, g_ref, t_ref, du_ref, dw_ref, dqd_ref, dkd_ref, da_ref, dgl_ref, dqkv_ref, dg_ref):
        h = pl.program_id(1)

        @pl.when(h == 0)
        def _():
            dg_ref[...] = jnp.zeros_like(dg_ref)

        lane = lax.broadcasted_iota(jnp.int32, (CHUNK, LANES), 1)
        chunks = [pl.ds(cidx * CHUNK, CHUNK) for cidx in range(CB)]
        gts = [g_ref[rows, :] for rows in chunks]
        t_saved = [t_ref[rows, :] for rows in chunks]
        _, vjp = jax.vjp(lambda *args: [o[:6] for o in _chunk_prep(*args, t_saved=t_saved)],
                         [q_ref[rows, :] for rows in chunks], [k_ref[rows, :] for rows in chunks],
                         [v_ref[rows, :] for rows in chunks], [_lane_col(gt, LANE_G + h) for gt in gts],
                         [_lane_col(gt, LANE_BETA + h) for gt in gts])
        dqs, dks, dvs, dgcs, dbcs = vjp([(du_ref[rows, :], dw_ref[rows, :], da_ref[rows, :], dqd_ref[rows, :],
                                          dkd_ref[rows, :], dgl_ref[cidx][:, 0:1]) for cidx, rows in enumerate(chunks)])
        for cidx, rows in enumerate(chunks):
            dq, dk, dv, dgc, dbc = dqs[cidx], dks[cidx], dvs[cidx], dgcs[cidx], dbcs[cidx]
            dqkv_ref[0, rows, :] = dq
            dqkv_ref[1, rows, :] = dk
            dqkv_ref[2, rows, :] = dv
            dg_ref[rows, :] += jnp.where(lane == LANE_G + h, dgc, 0.0) + jnp.where(lane == LANE_BETA + h, dbc, 0.0)

    qs, ks, vs, gs, a_s, gl_s = _prep_specs()
    return pl.pallas_call(
        body, name="gdn_prep_bwd", grid=(NCB, NGH), in_specs=[qs, ks, vs, gs, a_s, qs, qs, qs, qs, a_s, gl_s],
        out_specs=[pl.BlockSpec((3, CB * CHUNK, GHD), lambda i, h: (0, i, h)), gs],
        out_shape=[jax.ShapeDtypeStruct((3, S, DGDN), F32), jax.ShapeDtypeStruct((S, LANES), F32)],
        compiler_params=_cparams(),
    )(qkv, qkv, qkv, gates, t_inv, du, dw, dqd, dkd, da, dgl)


SCAN_HEADS = 2


def _scan_specs():
    wide = SCAN_HEADS * GHD
    hs = pl.BlockSpec((S, wide), lambda g: (0, g))
    a_s = pl.BlockSpec((SCAN_HEADS, S, CHUNK), lambda g: (g, 0, 0))
    gl_s = pl.BlockSpec((SCAN_HEADS, NCH, 1, LANES), lambda g: (g, 0, 0, 0))
    st_s = pl.BlockSpec((SCAN_HEADS, NCH, GHD, GHD), lambda g: (g, 0, 0, 0))
    gz_s = pl.BlockSpec((S, wide), lambda g: (0, BLK_GZ // SCAN_HEADS + g))
    mix_s = pl.BlockSpec((S, wide), lambda g: (0, NPAIR // SCAN_HEADS + g))
    return hs, a_s, gl_s, st_s, gz_s, mix_s


def _head_cols(hh):
    return slice(hh * GHD, (hh + 1) * GHD)


def _gdn_scan(u, w, qd, kd, a, gl, proj, w_norm, mix):
    heads = range(SCAN_HEADS)

    def body(u_ref, w_ref, qd_ref, kd_ref, a_ref, gl_ref, z_ref, wn_ref, mix_in, mix_ref, o_ref, st_ref):
        del mix_in

        def step(ci, states):
            rows = pl.ds(pl.multiple_of(ci * CHUNK, CHUNK), CHUNK)
            for hh in heads:
                st_ref[hh, ci] = states[hh]
            ws = [_dot(w_ref[rows, _head_cols(hh)], states[hh]) for hh in heads]
            qs = [_dot(qd_ref[rows, _head_cols(hh)], states[hh]) for hh in heads]
            vn = [u_ref[rows, _head_cols(hh)] - ws[hh] for hh in heads]
            av = [_dot(a_ref[hh, rows, :], vn[hh]) for hh in heads]
            kv = [_dot(kd_ref[rows, _head_cols(hh)], vn[hh], 0, 0) for hh in heads]
            for hh in heads:
                o_ref[rows, _head_cols(hh)] = qs[hh] + av[hh]
            return tuple(states[hh] * gl_ref[hh, ci] + kv[hh] for hh in heads)

        lax.fori_loop(0, NCH, step, (jnp.zeros((GHD, GHD), F32),) * SCAN_HEADS)
        for hh in heads:
            ov = o_ref[:, _head_cols(hh)]
            mix_ref[:, _head_cols(hh)] = (ov * _rms_scale(ov) * wn_ref[...] * _silu(z_ref[:, _head_cols(hh)])).astype(BF16)

    hs, a_s, gl_s, st_s, gz_s, mix_s = _scan_specs()
    return pl.pallas_call(
        body, name="gdn_scan", grid=(NGH // SCAN_HEADS,),
        in_specs=[hs, hs, hs, hs, a_s, gl_s, gz_s, pl.BlockSpec((1, GHD), lambda h: (0, 0)), pl.BlockSpec(memory_space=pl.ANY)],
        out_specs=[mix_s, hs, st_s],
        out_shape=[jax.ShapeDtypeStruct((S, D), BF16), jax.ShapeDtypeStruct((S, DGDN), F32),
                   jax.ShapeDtypeStruct((NGH, NCH, GHD, GHD), F32)],
        input_output_aliases={8: 0}, compiler_params=_cparams(),
    )(u, w, qd, kd, a, gl, proj, w_norm, mix)


def _gdn_scan_bwd(dmix, o, proj, w_norm, u, w, qd, kd, a, gl, states, dproj):
    def body(dy_ref, o_ref, z_ref, wn_ref, u_ref, w_ref, qd_ref, kd_ref, a_ref, gl_ref, st_ref, dproj_in,
             dz_ref, du_ref, dw_ref, dqd_ref, dkd_ref, da_ref, dgl_ref, dwn_ref, do_ref):
        del dproj_in
        heads = range(SCAN_HEADS)

        @pl.when(pl.program_id(0) == 0)
        def _():
            dwn_ref[...] = jnp.zeros_like(dwn_ref)

        wn = wn_ref[...]
        for hh in heads:
            c = _head_cols(hh)
            ov = o_ref[:, c]
            zv = z_ref[:, c]
            g = dy_ref[:, c]
            sig = _sigmoid(zv)
            dz_ref[:, c] = (g * (ov * _rms_scale(ov) * wn) * sig * (1.0 + zv * (1.0 - sig))).astype(BF16)
            do, dwt = _rms_bwd(ov, wn, g * zv * sig)
            do_ref[:, c] = do
            dwn_ref[...] += jnp.sum(dwt, axis=0, keepdims=True)

        def step(t, dstates):
            ci = NCH - 1 - t
            rows = pl.ds(pl.multiple_of(ci * CHUNK, CHUNK), CHUNK)
            cols = [_head_cols(hh) for hh in heads]
            state = [st_ref[hh, ci] for hh in heads]
            dov = [do_ref[rows, cols[hh]] for hh in heads]
            wv = [w_ref[rows, cols[hh]] for hh in heads]
            ws = [_dot(wv[hh], state[hh]) for hh in heads]
            adov = [_dot(a_ref[hh, rows, :], dov[hh], 0, 0) for hh in heads]
            kds = [_dot(kd_ref[rows, cols[hh]], dstates[hh]) for hh in heads]
            dqd = [_dot(dov[hh], state[hh], 1, 1) for hh in heads]
            qdo = [_dot(qd_ref[rows, cols[hh]], dov[hh], 0, 0) for hh in heads]
            vn = [u_ref[rows, cols[hh]] - ws[hh] for hh in heads]
            dvn = [adov[hh] + kds[hh] for hh in heads]
            da = [_dot(dov[hh], vn[hh], 1, 1) for hh in heads]
            dkd = [_dot(vn[hh], dstates[hh], 1, 1) for hh in heads]
            dwv = [_dot(dvn[hh], state[hh], 1, 1) for hh in heads]
            wdv = [_dot(wv[hh], dvn[hh], 0, 0) for hh in heads]
            for hh in heads:
                da_ref[hh, rows, :] = da[hh]
                dqd_ref[rows, cols[hh]] = dqd[hh]
                dkd_ref[rows, cols[hh]] = dkd[hh]
                dgl = jnp.sum(jnp.sum(dstates[hh] * state[hh], axis=1, keepdims=True), axis=0, keepdims=True)
                dgl_ref[hh, ci] = jnp.broadcast_to(dgl, (1, LANES))
                du_ref[rows, cols[hh]] = dvn[hh]
                dw_ref[rows, cols[hh]] = -dwv[hh]
            return tuple(dstates[hh] * gl_ref[hh, ci] + qdo[hh] - wdv[hh] for hh in heads)

        lax.fori_loop(0, NCH, step, (jnp.zeros((GHD, GHD), F32),) * SCAN_HEADS)

    hs, a_s, gl_s, st_s, gz_s, mix_s = _scan_specs()
    once = pl.Buffered(buffer_count=1)
    hs1, a1, st1 = [pl.BlockSpec(s.block_shape, s.index_map, pipeline_mode=once) for s in (hs, a_s, st_s)]
    vec = pl.BlockSpec((1, GHD), lambda h: (0, 0))
    tok = jax.ShapeDtypeStruct((S, DGDN), F32)
    return pl.pallas_call(
        body, name="gdn_scan_bwd", grid=(NGH // SCAN_HEADS,),
        in_specs=[mix_s, hs, gz_s, vec, hs, hs, hs, hs, a1, gl_s, st1, pl.BlockSpec(memory_space=pl.ANY)],
        out_specs=[gz_s, hs1, hs1, hs1, hs1, a1, gl_s, vec],
        out_shape=[jax.ShapeDtypeStruct((S, DPROJ_PAD), BF16), tok, tok, tok, tok,
                   jax.ShapeDtypeStruct((NGH, S, CHUNK), F32), jax.ShapeDtypeStruct((NGH, NCH, 1, LANES), F32),
                   jax.ShapeDtypeStruct((1, GHD), F32)],
        input_output_aliases={11: 0}, scratch_shapes=[pltpu.VMEM((S, SCAN_HEADS * GHD), F32)], compiler_params=_cparams(),
    )(dmix, o, proj, w_norm, u, w, qd, kd, a, gl, states, dproj)


def _place():
    return lax.axis_index("x"), lax.axis_index("y"), lax.axis_index("c")


def _other_chips(x, y):
    return [(1 - x, y), (x, 1 - y), (1 - x, 1 - y)]


HBM = pl.BlockSpec(memory_space=pltpu.HBM)
VMEM = pl.BlockSpec(memory_space=pltpu.VMEM)


def _half_rows(ref_or_rows, half):
    rows = ref_or_rows // 2
    return pl.ds(pl.multiple_of(half * rows, rows), rows)


class _Exchange:
    def __init__(self, inputs, out_shape, n_sems, start, finish):
        self.inputs, self.out_shape, self.n_sems, self.start, self.finish = inputs, out_shape, n_sems, start, finish

    def sem_shapes(self):
        return [pltpu.SemaphoreType.DMA((self.n_sems,)), pltpu.SemaphoreType.DMA((self.n_sems,))]

    def split(self, refs):
        n_in, n_out = len(self.inputs), len(self.out_shape)
        return refs[:n_in], refs[n_in:n_in + n_out], refs[n_in + n_out], refs[n_in + n_out + 1]


def _run_exchange(ex, name):
    def body(*refs):
        parts = ex.split(refs)
        ex.start(*parts)
        ex.finish(*parts)

    return pl.pallas_call(
        body, name=name, in_specs=[HBM] * len(ex.inputs), out_specs=[HBM] * len(ex.out_shape), out_shape=ex.out_shape,
        scratch_shapes=ex.sem_shapes(), compiler_params=_cparams(),
    )(*ex.inputs)


def _allgather_exchange(shards, whole=()):
    n, nw = len(shards), len(whole)

    def plan(src, outs, send_sems, recv_sems):
        x, y, c = _place()
        chips = _other_chips(x, y)
        chip_ids = [2 * ch[0] + ch[1] for ch in chips]

        def copy(a, k, chip_index, half, to, from_src):
            rows = _half_rows(src[a].shape[0], half)
            dst = outs[a].at[chip_index, rows]
            return pltpu.make_async_remote_copy(
                src_ref=src[a].at[rows] if from_src else dst, dst_ref=dst, send_sem=send_sems.at[6 * a + k],
                recv_sem=recv_sems.at[6 * a + k], device_id=to, device_id_type=MESH)

        def whole_copy(b, k, chip_index, to):
            return pltpu.make_async_remote_copy(
                src_ref=src[n + b], dst_ref=outs[n + b].at[chip_index], send_sem=send_sems.at[6 * n + 3 * b + k],
                recv_sem=recv_sems.at[6 * n + 3 * b + k], device_id=to, device_id_type=MESH)

        me, sibling = (x, y, c), (x, y, 1 - c)
        first = [copy(a, j, 2 * x + y, c, (*chips[j], c), True) for a in range(n) for j in range(3)]
        first += [whole_copy(b, j, 2 * x + y, (*chips[j], c)) for b in range(nw) for j in range(3)]
        landing = [copy(a, j, chip_ids[j], c, me, False) for a in range(n) for j in range(3)]
        passed = [copy(a, 3 + j, chip_ids[j], c, sibling, False) for a in range(n) for j in range(3)]
        arriving = [copy(a, 3 + j, chip_ids[j], 1 - c, me, False) for a in range(n) for j in range(3)]
        arriving += [whole_copy(b, j, chip_ids[j], me) for b in range(nw) for j in range(3)]
        return first, landing, passed, arriving

    def start(*refs):
        for cp in plan(*refs)[0]:
            cp.start()

    def finish(*refs):
        first, landing, passed, arriving = plan(*refs)
        for lands, onward in zip(landing, passed):
            lands.wait_recv()
            onward.start()
        for cp in arriving:
            cp.wait_recv()
        for cp in first + passed:
            cp.wait_send()

    out_shape = [jax.ShapeDtypeStruct((NCHIP,) + s.shape, s.dtype) for s in list(shards) + list(whole)]
    return _Exchange(list(shards) + list(whole), out_shape, 6 * n + 3 * nw, start, finish)


def _with_own(gathered, own):
    x, y, _ = _place()
    return lax.dynamic_update_index_in_dim(gathered, own, 2 * x + y, axis=0)


def _simple_exchange(inputs, out_shape, copies_of):
    def start(*refs):
        for cp in copies_of(*refs):
            cp.start()

    def finish(*refs):
        for cp in copies_of(*refs):
            cp.wait()

    return _Exchange(list(inputs), out_shape, len(out_shape) * 3, start, finish)


def _pair_exchange(grads):
    def copies_of(src, outs, send_sems, recv_sems):
        x, y, c = _place()
        return [pltpu.make_async_remote_copy(
            src_ref=src[a].at[:, _half_rows(src[a].shape[1], 1 - c)], dst_ref=outs[a], send_sem=send_sems.at[a],
            recv_sem=recv_sems.at[a], device_id=(x, y, 1 - c), device_id_type=MESH) for a in range(len(src))]

    return _simple_exchange(
        grads, [jax.ShapeDtypeStruct((g.shape[0], g.shape[1] // 2, g.shape[2]), g.dtype) for g in grads], copies_of)


def _pair_sum(grads, theirs, name):
    n = len(grads)

    def body(*refs):
        south = lax.axis_index("c") == 0
        for a in range(n):
            g = refs[a][...]
            half = g.shape[0] // 2
            mine = jnp.where(south, g[:half], g[half:])
            refs[2 * n + a][...] = (mine.astype(F32) + refs[n + a][...].astype(F32)).astype(BF16)

    def specs(arrs):
        return [pl.BlockSpec((None,) + g.shape[1:], lambda j: (j, 0, 0)) for g in arrs]

    return pl.pallas_call(
        body, name=name, grid=(NCHIP,), in_specs=specs(grads) + specs(theirs), out_specs=specs(theirs),
        out_shape=[jax.ShapeDtypeStruct(g.shape, BF16) for g in theirs], compiler_params=_cparams(),
    )(*grads, *theirs)


def _chip_exchange(parts):
    def copies_of(src, outs, send_sems, recv_sems):
        x, y, c = _place()
        return [pltpu.make_async_remote_copy(
            src_ref=src[a].at[2 * chip[0] + chip[1]], dst_ref=outs[a].at[k], send_sem=send_sems.at[3 * a + k],
            recv_sem=recv_sems.at[3 * a + k], device_id=(*chip, c), device_id_type=MESH)
            for a in range(len(src)) for k, chip in enumerate(_other_chips(x, y))]

    return _simple_exchange(parts, [jax.ShapeDtypeStruct((NCHIP - 1,) + p.shape[1:], p.dtype) for p in parts], copies_of)


def _chip_sum(parts, received, exchange=None):
    n = len(parts)
    steps = 4
    ex_in, ex_in_specs, ex_out_specs, ex_out_shape, ex_scratch = _hosted(exchange)

    def body(*refs):
        ex_refs = refs[2 * n:2 * n + len(ex_in)] + refs[3 * n + len(ex_in):]
        if exchange is not None:
            @pl.when(pl.program_id(0) == 0)
            def _():
                exchange.start(*exchange.split(ex_refs))

        chip = 2 * lax.axis_index("x") + lax.axis_index("y")
        for a in range(n):
            p, r = refs[a], refs[n + a]
            own = jnp.where(chip == 0, p[0], jnp.where(chip == 1, p[1], jnp.where(chip == 2, p[2], p[3])))
            refs[2 * n + len(ex_in) + a][...] = ((own.astype(F32) + r[0].astype(F32)) + r[1].astype(F32)) + r[2].astype(F32)

        if exchange is not None:
            @pl.when(pl.program_id(0) == steps - 1)
            def _():
                exchange.finish(*exchange.split(ex_refs))

    def specs(arrs):
        return [pl.BlockSpec((g.shape[0], g.shape[1] // steps, g.shape[2]), lambda i: (0, i, 0)) for g in arrs]

    out_specs = [pl.BlockSpec((g.shape[1] // steps, g.shape[2]), lambda i: (i, 0)) for g in parts]
    res = pl.pallas_call(
        body, name="grads_chip_sum", grid=(steps,), in_specs=specs(parts) + specs(received) + ex_in_specs,
        out_specs=out_specs + ex_out_specs, out_shape=[jax.ShapeDtypeStruct(g.shape[1:], F32) for g in parts] + ex_out_shape,
        scratch_shapes=ex_scratch, compiler_params=_cparams(),
    )(*parts, *received, *ex_in)
    return res[:n], res[n:]


def _pair_share(halves):
    def copies_of(src, outs, send_sems, recv_sems):
        x, y, c = _place()
        return [pltpu.make_async_remote_copy(
            src_ref=src[a], dst_ref=outs[a], send_sem=send_sems.at[a], recv_sem=recv_sems.at[a],
            device_id=(x, y, 1 - c), device_id_type=MESH) for a in range(len(src))]

    return _simple_exchange(halves, [jax.ShapeDtypeStruct(h.shape, F32) for h in halves], copies_of)


def _adamw_math(w, g, m, v):
    nm = ADAM_B1 * m + (1.0 - ADAM_B1) * g
    nv = ADAM_B2 * v + (1.0 - ADAM_B2) * jnp.square(g)
    m_hat = nm / (1.0 - ADAM_B1 ** ADAM_STEP)
    v_hat = nv / (1.0 - ADAM_B2 ** ADAM_STEP)
    return -ADAM_LR * (m_hat / (jnp.sqrt(v_hat) + ADAM_EPS) + ADAM_WD * w), nm, nv


def _adamw_big(ws, g_mine, g_theirs, ms, vs, exchange=None):
    n = len(ws)
    steps = 8
    ex_in, ex_in_specs, ex_out_specs, ex_out_shape, ex_scratch = _hosted(exchange)

    def body(*refs):
        ex_refs = refs[5 * n:5 * n + len(ex_in)] + refs[9 * n + len(ex_in):]
        outs = refs[5 * n + len(ex_in):9 * n + len(ex_in)]
        if exchange is not None:
            @pl.when(pl.program_id(0) == 0)
            def _():
                exchange.start(*exchange.split(ex_refs))

        own_half = (pl.program_id(0) // (steps // 2)) == lax.axis_index("c")
        for a in range(n):
            g = jnp.where(own_half, refs[n + a][...], refs[2 * n + a][...])
            d, nm, nv = _adamw_math(refs[a][...], g, refs[3 * n + a][...], refs[4 * n + a][...])
            outs[a][...] = g
            outs[n + a][...] = d
            outs[2 * n + a][...] = nm
            outs[3 * n + a][...] = nv

        if exchange is not None:
            @pl.when(pl.program_id(0) == steps - 1)
            def _():
                exchange.finish(*exchange.split(ex_refs))

    specs = [pl.BlockSpec((w.shape[0] // steps, w.shape[1]), lambda i: (i, 0)) for w in ws]
    half_specs = [pl.BlockSpec((g.shape[0] // (steps // 2), g.shape[1]), lambda i: (i % (steps // 2), 0)) for g in g_mine]
    shapes = [jax.ShapeDtypeStruct(w.shape, F32) for w in ws]
    res = pl.pallas_call(
        body, name="adamw_big", grid=(steps,), in_specs=specs + half_specs * 2 + specs * 2 + ex_in_specs,
        out_specs=specs * 4 + ex_out_specs, out_shape=shapes * 4 + ex_out_shape, scratch_shapes=ex_scratch,
        compiler_params=_cparams(),
    )(*ws, *g_mine, *g_theirs, *ms, *vs, *ex_in)
    return res[:n], res[n:2 * n], res[2 * n:3 * n], res[3 * n:4 * n], res[4 * n:]


def _adamw_in(w, g_mine, g_theirs, m, v):
    half = D // 2

    def body(w_ref, gm_ref, gt_ref, m_ref, v_ref, g_out, d_out, nm_out, nv_out, g_ref):
        south = lax.axis_index("c") == 0
        g_ref[0:half, :] = jnp.where(south, gm_ref[...], gt_ref[...])
        g_ref[half:D, :] = jnp.where(south, gt_ref[...], gm_ref[...])
        g = g_ref[0:CW, :]
        d, nm, nv = _adamw_math(w_ref[...], g, m_ref[...], v_ref[...])
        g_out[...] = g
        d_out[...] = d
        nm_out[...] = nm
        nv_out[...] = nv

    spec = pl.BlockSpec((CW, LANES), lambda i: (0, i))
    half_spec = pl.BlockSpec((half, LANES), lambda i: (0, i))
    return pl.pallas_call(
        body, name="adamw_in", grid=(D // LANES,), in_specs=[spec, half_spec, half_spec, spec, spec], out_specs=[spec] * 4,
        out_shape=[jax.ShapeDtypeStruct((CW, D), F32)] * 4, scratch_shapes=[pltpu.VMEM((D, LANES), F32)],
        compiler_params=_cparams(),
    )(w, g_mine, g_theirs, m, v)


NORM_NAMES = ("pre_mix_norm", "post_mix_norm", "pre_mlp_norm", "post_mlp_norm")
SMALL_NAMES = NORM_NAMES + ("gdn_conv_w", "fox_f_bias", "gdn_dt_bias", "gdn_a_log", "fox_out_norm", "gdn_out_norm")
CONV_COLS = 3 * DGDN // NCHIP


def _small_gather(d_norms, d_conv, sums, d_fox_norm, d_gdn_norm):
    n_remote = 5 * (NDEV - 1)

    def copies_of(src, outs, send_sems, recv_sems):
        x, y, c = _place()
        me = 4 * x + 2 * y + c

        def from_me(chip_index):
            cols = pl.ds(pl.multiple_of(chip_index * CONV_COLS, LANES), CONV_COLS)
            return [src[0], src[1].at[:, cols], src[2], src[3], src[4]]

        local = [pltpu.make_async_copy(s, outs[a].at[me], send_sems.at[n_remote + a]) for a, s in enumerate(from_me(2 * x + y))]
        remote = []
        for k in range(1, NDEV):
            px, py, pc = x ^ ((k >> 2) & 1), y ^ ((k >> 1) & 1), c ^ (k & 1)
            remote += [pltpu.make_async_remote_copy(
                src_ref=s, dst_ref=outs[a].at[me], send_sem=send_sems.at[5 * (k - 1) + a],
                recv_sem=recv_sems.at[5 * (k - 1) + a], device_id=(px, py, pc), device_id_type=MESH)
                for a, s in enumerate(from_me(2 * px + py))]
        return local + remote

    def start(*refs):
        for cp in copies_of(*refs):
            cp.start()

    def finish(*refs):
        for cp in copies_of(*refs):
            cp.wait()

    shapes = [(4, D), (CONV_K, CONV_COLS), (8, LANES), (1, LANES), (1, LANES)]
    return _Exchange([d_norms, d_conv, sums, d_fox_norm, d_gdn_norm],
                     [jax.ShapeDtypeStruct((NDEV,) + s, F32) for s in shapes], n_remote + 5, start, finish)


def _small_adamw(gathered, ws, ms, vs):
    n = len(SMALL_NAMES)

    def body(*refs):
        def total(buf):
            acc = buf[0]
            for i in range(1, NDEV):
                acc = acc + buf[i]
            return acc

        t_norms, t_conv, t_sums, t_fn, t_gn = [total(r) for r in refs[:5]]
        w_refs, m_refs, v_refs = refs[5:5 + n], refs[5 + n:5 + 2 * n], refs[5 + 2 * n:5 + 3 * n]
        outs = refs[5 + 3 * n:]
        grads = [t_norms[i:i + 1, :] for i in range(4)] + [
            t_conv, t_sums[0:1, 0:NFH], t_sums[1:2, 0:NGH], t_sums[2:3, 0:NGH], t_fn[:, 0:FHD], t_gn]
        for a in range(n):
            d, nm, nv = _adamw_math(w_refs[a][...], grads[a], m_refs[a][...], v_refs[a][...])
            outs[a][...] = grads[a]
            outs[n + a][...] = d
            outs[2 * n + a][...] = nm
            outs[3 * n + a][...] = nv

    def whole(arr):
        return pl.BlockSpec(arr.shape, lambda i: (0,) * arr.ndim)

    res = pl.pallas_call(
        body, name="small_adamw", grid=(1,), in_specs=[whole(t) for t in gathered] + [whole(w) for w in ws] * 3,
        out_specs=[whole(w) for w in ws] * 4, out_shape=[jax.ShapeDtypeStruct(w.shape, F32) for w in ws] * 4,
        compiler_params=_cparams(),
    )(*gathered, *ws, *ms, *vs)
    return res[:n], res[n:2 * n], res[2 * n:3 * n], res[3 * n:]


CW = DPROJ // NCHIP
PROJ_RUNS = tuple((part * DFOX + hp * LANES, part * DFOX + (hp + 1) * LANES, (3 * hp + part) * LANES)
                  for hp in range(NPAIR) for part in range(3)) + (
    (1536, 1544, BLK_SMALL * LANES), (1544, 3080, BLK_GDN * LANES), (3080, 3088, BLK_SMALL * LANES + 8),
    (3088, 3600, BLK_GZ * LANES))


def _proj_pieces():
    pieces = []
    for lo, hi, at in PROJ_RUNS:
        while lo < hi:
            j = lo // CW
            end = min(hi, (j + 1) * CW)
            pieces.append((j, lo - j * CW, at, end - lo))
            at, lo = at + end - lo, end
    return pieces


RT = 256


def _to_padded_rows(gathered):
    def body(src_ref, out_ref, blocks_ref, rows_ref):
        blocks_ref[...] = src_ref[...].astype(F32)
        rows_ref[...] = jnp.zeros_like(rows_ref)
        for j, start, at, n in _proj_pieces():
            rows_ref[at:at + n, :] = blocks_ref[j, start:start + n, :]
        out_ref[...] = rows_ref[...].astype(out_ref.dtype)

    return pl.pallas_call(
        body, name="proj_rows_in", grid=(D // RT,), in_specs=[pl.BlockSpec((NCHIP, D, RT), lambda i: (0, 0, i))],
        out_specs=pl.BlockSpec((DPROJ_PAD, RT), lambda i: (0, i)), out_shape=jax.ShapeDtypeStruct((DPROJ_PAD, D), gathered.dtype),
        scratch_shapes=[pltpu.VMEM((NCHIP, D, RT), F32), pltpu.VMEM((DPROJ_PAD, RT), F32)], compiler_params=_cparams(),
    )(gathered)


def _from_padded_rows(w):
    def body(src_ref, out_ref, rows_ref, blocks_ref):
        rows_ref[...] = src_ref[...].astype(F32)
        blocks_ref[...] = jnp.zeros_like(blocks_ref)
        for j, start, at, n in _proj_pieces():
            blocks_ref[j, start:start + n, :] = rows_ref[at:at + n, :]
        out_ref[...] = blocks_ref[...].astype(out_ref.dtype)

    return pl.pallas_call(
        body, name="proj_rows_out", grid=(D // RT,), in_specs=[pl.BlockSpec((DPROJ_PAD, RT), lambda i: (0, i))],
        out_specs=pl.BlockSpec((NCHIP, D, RT), lambda i: (0, 0, i)), out_shape=jax.ShapeDtypeStruct((NCHIP, D, D), w.dtype),
        scratch_shapes=[pltpu.VMEM((DPROJ_PAD, RT), F32), pltpu.VMEM((NCHIP, D, RT), F32)], compiler_params=_cparams(),
    )(w)


def _local_step(x, target, first_weights, late_weights, reduce_late, reduce_in, pre_mix_norm, fox_f_bias, fox_out_norm,
                gdn_a_log, gdn_dt_bias, gdn_out_norm, post_mix_norm, pre_mlp_norm, post_mlp_norm):
    bias_vec = jnp.zeros((1, LANES), F32).at[0, 0:NFH].set(fox_f_bias).at[0, LANE_G:LANE_G + NGH].set(gdn_dt_bias)
    alog_vec = jnp.zeros((1, LANES), F32).at[0, LANE_G:LANE_G + NGH].set(gdn_a_log)
    w2 = jnp.concatenate([fox_out_norm, fox_out_norm], axis=1)

    h, first = _pre_norm(x, pre_mix_norm, exchange=first_weights[0])
    win_p, conv_w = first_weights[1](first)
    proj = _matmul(h, win_p, tb=True, tm=2048, tn=768, tk=1024, name="mm_proj")
    gates = _gates(proj, bias_vec, alog_vec)
    mix, fox_o, lse, late_a = _fox_fwd(proj, gates, w2, exchange=late_weights[0])
    qkv = _gdn_pre(proj, conv_w)
    (u, w, qd, kd, a_intra, gl, t_inv), late_b = _gdn_prep(qkv, gates, exchange=late_weights[1])
    wout, wup3, wdown = late_weights[2](late_a, late_b)
    mix, gdn_raw, states = _gdn_scan(u, w, qd, kd, a_intra, gl, proj, gdn_out_norm, mix)
    mixed = _matmul(mix, wout, tm=2048, tk=1024, name="mm_out")
    x1, h2 = _post_mix(x, mixed, post_mix_norm, pre_mlp_norm)

    def relu2(acc):
        r = jnp.maximum(acc, 0.0)
        return r, r * r

    up_relu, act = _matmul(h2, wup3, b3=True, tm=1024, tn=1024, tk=1024, out_dtypes=(BF16, BF16), epilogue=relu2,
                           name="mm_up")
    y = _matmul(act, wdown, tm=2048, tk=1024, name="mm_down")
    dx2, dy, d_post_mlp, loss_row = _loss_head(x1, y, post_mlp_norm, target)

    dwdown = _matmul(act, dy, ta=True, tm=1024, tn=1024, tk=2048, out_dtypes=(BF16,), name="mm_dwdown")

    def relu2_bwd(acc, r):
        return (acc * 2.0 * r.astype(F32),)

    dup = _matmul(dy, wdown, tb=True, tm=1024, tn=1024, tk=1024, out_dtypes=(BF16,), extra=(up_relu,), epilogue=relu2_bwd,
                  name="mm_dact")
    dwup3 = _matmul(h2, dup, ta=True, tm=1024, tn=1024, tk=2048, out_dtypes=(BF16,), o3=True, name="mm_dwup")
    dh2 = _matmul(dup, wup3, tb=True, b3=True, tm=2048, tk=1024, name="mm_dh2")
    dx1, dmixed, d_pre_mlp, d_post_mix = _mid_bwd(dh2, x1, pre_mlp_norm, dx2, mixed, post_mix_norm)
    dwout = _matmul(mix, dmixed, ta=True, tm=1024, tn=1024, tk=2048, out_dtypes=(BF16,), name="mm_dwout")
    dmix = _matmul(dmixed, wout, tb=True, tm=2048, tk=1024, name="mm_dmix")

    dfox, delta, d_fox_norm, from_sibling = _fox_norm_bwd(fox_o, dmix, w2, exchange=reduce_late[0](dwout, dwup3, dwdown))
    dproj, dcum_fox, reduced_late = _fox_bwd(proj, dfox, gates, lse, delta, exchange=reduce_late[1](from_sibling))
    dproj, du, dw, dqd, dkd, da, dgl, d_gdn_norm = _gdn_scan_bwd(dmix, gdn_raw, proj, gdn_out_norm, u, w, qd, kd,
                                                                 a_intra, gl, states, dproj)
    dqkv, dgates_gdn = _gdn_prep_bwd(qkv, gates, t_inv, du, dw, dqd, dkd, da, dgl)
    dproj, d_conv = _gdn_pre_bwd(proj, conv_w, dqkv, dproj)
    dproj, sums = _gates_bwd(proj, bias_vec, alog_vec, dgates_gdn, dcum_fox, dproj)

    dwin_p = _matmul(dproj, h, ta=True, tm=1280, tn=1024, tk=2048, out_dtypes=(BF16,), name="mm_dwin")
    exchange_in = reduce_in(dwin_p)
    dh = _matmul(dproj, win_p, tm=2048, tk=1280, name="mm_dh", exchange=exchange_in)
    dh, reduced_in = dh if exchange_in is not None else (dh, [])
    grad_x, d_pre_mix = _pre_norm_bwd(dh, x, pre_mix_norm, dx1)

    d_norms = jnp.concatenate([d_pre_mix, d_post_mix, d_pre_mlp, d_post_mlp], axis=0)
    return loss_row[0, 0], grad_x, (d_norms, d_conv, sums, d_fox_norm, d_gdn_norm), reduced_late, reduced_in


def kernel(x, pre_mix_norm, w_in, fox_f_bias, fox_out_norm, gdn_conv_w, gdn_a_log, gdn_dt_bias, gdn_out_norm, w_out, post_mix_norm, pre_mlp_norm, w_up, w_down, post_mlp_norm, loss_target, m_pre_mix_norm, m_w_in, m_fox_f_bias, m_fox_out_norm, m_gdn_conv_w, m_gdn_a_log, m_gdn_dt_bias, m_gdn_out_norm, m_w_out, m_post_mix_norm, m_pre_mlp_norm, m_w_up, m_w_down, m_post_mlp_norm, v_pre_mix_norm, v_w_in, v_fox_f_bias, v_fox_out_norm, v_gdn_conv_w, v_gdn_a_log, v_gdn_dt_bias, v_gdn_out_norm, v_w_out, v_post_mix_norm, v_pre_mlp_norm, v_w_up, v_w_down, v_post_mlp_norm):
    weights = dict(pre_mix_norm=pre_mix_norm, w_in=w_in, fox_f_bias=fox_f_bias, fox_out_norm=fox_out_norm, gdn_conv_w=gdn_conv_w,
                   gdn_a_log=gdn_a_log, gdn_dt_bias=gdn_dt_bias, gdn_out_norm=gdn_out_norm, w_out=w_out, post_mix_norm=post_mix_norm,
                   pre_mlp_norm=pre_mlp_norm, w_up=w_up, w_down=w_down, post_mlp_norm=post_mlp_norm)
    m_in = dict(pre_mix_norm=m_pre_mix_norm, w_in=m_w_in, fox_f_bias=m_fox_f_bias, fox_out_norm=m_fox_out_norm, gdn_conv_w=m_gdn_conv_w,
                gdn_a_log=m_gdn_a_log, gdn_dt_bias=m_gdn_dt_bias, gdn_out_norm=m_gdn_out_norm, w_out=m_w_out, post_mix_norm=m_post_mix_norm,
                pre_mlp_norm=m_pre_mlp_norm, w_up=m_w_up, w_down=m_w_down, post_mlp_norm=m_post_mlp_norm)
    v_in = dict(pre_mix_norm=v_pre_mix_norm, w_in=v_w_in, fox_f_bias=v_fox_f_bias, fox_out_norm=v_fox_out_norm, gdn_conv_w=v_gdn_conv_w,
                gdn_a_log=v_gdn_a_log, gdn_dt_bias=v_gdn_dt_bias, gdn_out_norm=v_gdn_out_norm, w_out=v_w_out, post_mix_norm=v_post_mix_norm,
                pre_mlp_norm=v_pre_mlp_norm, w_up=v_w_up, w_down=v_w_down, post_mlp_norm=v_post_mlp_norm)
    order_w = ("pre_mix_norm", "w_in", "fox_f_bias", "fox_out_norm", "gdn_conv_w", "gdn_a_log", "gdn_dt_bias", "gdn_out_norm", "w_out",
               "post_mix_norm", "pre_mlp_norm", "w_up", "w_down", "post_mlp_norm")
    big = ("w_in", "w_out", "w_up", "w_down")

    def row(v):
        return v if v.ndim == 2 else v.reshape(1, -1)

    win_shard = jnp.pad(w_in.T.astype(BF16), ((0, D - CW), (0, 0)))

    def resolve_first(gathered):
        win_g, conv_g = gathered
        return (_to_padded_rows(_with_own(win_g, win_shard)),
                _with_own(conv_g, gdn_conv_w).transpose(1, 0, 2).reshape(CONV_K, 3 * DGDN))

    late_shards = [weights[n].astype(BF16) for n in big[1:]]

    def resolve_late(gathered_a, gathered_b):
        wout_g, wup3, wdown_g = [_with_own(g, own) for g, own in zip(list(gathered_a) + list(gathered_b), late_shards)]
        return wout_g.reshape(D, D), wup3, wdown_g.reshape(DFF, D)

    pair_sums, late_blocks = {}, []

    def chip_exchange_of(names, blocks, theirs):
        for n, s in zip(names, _pair_sum(blocks, theirs, "grads_pair_sum_" + names[0])):
            pair_sums[n] = s
        return _chip_exchange([pair_sums[n] for n in names])

    def late_pair_exchange(dwout, dwup3, dwdown):
        late_blocks.extend([dwout.reshape(NCHIP, D // NCHIP, D), dwup3, dwdown.reshape(NCHIP, DFF // NCHIP, D)])
        return _pair_exchange(late_blocks)

    def late_chip_exchange(theirs):
        return chip_exchange_of(big[1:], late_blocks, theirs)

    def reduce_in(dwin_p):
        blocks = [_from_padded_rows(dwin_p)]
        return chip_exchange_of(big[:1], blocks, _run_exchange(_pair_exchange(blocks), "grads_pair_exchange_w_in"))

    loss_local, grad_x, small, received_late, received_in = _local_step(
        x[0], loss_target[0], (_allgather_exchange([win_shard], whole=[gdn_conv_w]), resolve_first),
        (_allgather_exchange(late_shards[:2]), _allgather_exchange(late_shards[2:]), resolve_late),
        (late_pair_exchange, late_chip_exchange), reduce_in, row(pre_mix_norm), fox_f_bias, row(fox_out_norm),
        gdn_a_log, gdn_dt_bias,
        row(gdn_out_norm), row(post_mix_norm), row(pre_mlp_norm), row(post_mlp_norm))
    loss = lax.psum(loss_local, ("x", "y", "c"))

    g_mine, small_gathered = _chip_sum([pair_sums[n] for n in big], list(received_in) + list(received_late),
                                       exchange=_small_gather(*small))
    g_theirs = _run_exchange(_pair_share(g_mine), "grads_pair_share")

    g_big, d_big, nm_big, nv_big, _ = _adamw_big(
        [weights[n] for n in big[1:]], g_mine[1:], g_theirs[1:], [m_in[n] for n in big[1:]], [v_in[n] for n in big[1:]])
    in_t = _adamw_in(w_in.T, g_mine[0], g_theirs[0], m_w_in.T, v_w_in.T)
    g_small, d_small, nm_small, nv_small = _small_adamw(
        small_gathered, [row(weights[n]) for n in SMALL_NAMES], [row(m_in[n]) for n in SMALL_NAMES],
        [row(v_in[n]) for n in SMALL_NAMES])

    grads, delta, new_m, new_v = {}, {}, {}, {}
    grads["w_in"], delta["w_in"], new_m["w_in"], new_v["w_in"] = [t.T for t in in_t]
    for i, n in enumerate(big[1:]):
        grads[n], delta[n], new_m[n], new_v[n] = g_big[i], d_big[i], nm_big[i], nv_big[i]
    for i, n in enumerate(SMALL_NAMES):
        shape = weights[n].shape
        grads[n], delta[n], new_m[n], new_v[n] = (g_small[i].reshape(shape), d_small[i].reshape(shape),
                                                  nm_small[i].reshape(shape), nv_small[i].reshape(shape))
    return (loss, grad_x[None], *[grads[n] for n in order_w], *[delta[n] for n in order_w], *[new_m[n] for n in order_w],
            *[new_v[n] for n in order_w])
```

```python
import jax
import jax.numpy as jnp
from jax import lax
from jax.experimental import pallas as pl
from jax.experimental.pallas import tpu as pltpu

F32 = jnp.float32
BF16 = jnp.bfloat16
MESH = pl.DeviceIdType.MESH

S = 2048
D = 1024
NFH, FHD = 8, 64
NPAIR = NFH // 2
NGH, GHD = 4, 128
DFOX = NFH * FHD
DGDN = NGH * GHD
CHUNK = 64
NCH = S // CHUNK
CONV_K = 4
DFF = 4 * D
EPS = 1e-6
DPROJ = 3600
LANES = 128
DPROJ_PAD = 3840
BLK_GDN = 12
BLK_GZ = 24
BLK_SMALL = 28
NCHIP = 4
NDEV = 8
VMEM_LIMIT = 56 * 1024 * 1024

ADAM_LR = 0.001
ADAM_B1 = 0.9
ADAM_B2 = 0.999
ADAM_EPS = 1e-08
ADAM_WD = 0.01
ADAM_STEP = 10


def _cparams(**kw):
    return pltpu.CompilerParams(vmem_limit_bytes=VMEM_LIMIT, **kw)


def _dn(ca, cb):
    return (((ca,), (cb,)), ((), ()))


def _dot(a, b, ca=1, cb=0):
    return lax.dot_general(a.astype(BF16), b.astype(BF16), _dn(ca, cb), preferred_element_type=F32)


def _hdot(a, b, ca=1, cb=0):
    return lax.dot_general(a.astype(F32), b.astype(F32), _dn(ca, cb), precision=lax.Precision.HIGHEST,
                           preferred_element_type=F32)


def _dot3(a, b, ca=1, cb=0):
    a_hi, b_hi = a.astype(BF16), b.astype(BF16)
    a_lo, b_lo = (a - a_hi.astype(F32)).astype(BF16), (b - b_hi.astype(F32)).astype(BF16)
    dn = _dn(ca, cb)
    return (lax.dot_general(a_hi, b_hi, dn, preferred_element_type=F32)
            + (lax.dot_general(a_hi, b_lo, dn, preferred_element_type=F32)
               + lax.dot_general(a_lo, b_hi, dn, preferred_element_type=F32)))


@jax.custom_vjp
def _mm_nn(a, b):
    return _dot(a, b, 1, 0)


def _mm_nn_fwd(a, b):
    return _dot(a, b, 1, 0), (a, b)


def _mm_nn_bwd(res, g):
    a, b = res
    return _dot(g, b, 1, 1), _dot(a, g, 0, 0)


_mm_nn.defvjp(_mm_nn_fwd, _mm_nn_bwd)


@jax.custom_vjp
def _mm_nt(a, b):
    return _dot(a, b, 1, 1)


def _mm_nt_fwd(a, b):
    return _dot(a, b, 1, 1), (a, b)


def _mm_nt_bwd(res, g):
    a, b = res
    return _dot(g, b, 1, 0), _dot(g, a, 0, 0)


_mm_nt.defvjp(_mm_nt_fwd, _mm_nt_bwd)


@jax.custom_vjp
def _saved_inverse(m, t_inv):
    del m
    return t_inv


def _saved_inverse_fwd(m, t_inv):
    del m
    return t_inv, t_inv


def _saved_inverse_bwd(t_inv, g):
    return -_dot3(_dot3(t_inv, g, 0, 0), t_inv, 1, 1), jnp.zeros_like(t_inv)


_saved_inverse.defvjp(_saved_inverse_fwd, _saved_inverse_bwd)


def _sigmoid(z):
    return 1.0 / (1.0 + jnp.exp(-z))


def _softplus(z):
    return jnp.maximum(z, 0.0) + jnp.log(1.0 + jnp.exp(-jnp.abs(z)))


def _silu(z):
    return z * _sigmoid(z)


def _rms_scale(x):
    return lax.rsqrt(jnp.mean(x * x, axis=-1, keepdims=True) + EPS)


def _rms_bwd(x, w, g):
    r = _rms_scale(x)
    gw = g * w
    dx = r * gw - x * (r * r * r) * jnp.mean(gw * x, axis=-1, keepdims=True)
    return dx, g * x * r


def _matmul(a, b, *, name, ta=False, tb=False, tm=512, tn=512, tk=512, out_dtypes=(F32,), b3=False, o3=False,
            extra=(), epilogue=None, exchange=None):
    m, k = (a.shape[1], a.shape[0]) if ta else a.shape
    if b3:
        n = b.shape[1] if tb else b.shape[0] * b.shape[2]
        kb = b.shape[0] * b.shape[2] if tb else b.shape[1]
    else:
        n, kb = (b.shape[0], b.shape[1]) if tb else (b.shape[1], b.shape[0])
    assert kb == k, (name, kb, k)
    tm, tn, tk = min(tm, m), min(tn, n), min(tk, k)
    assert m % tm == 0 and n % tn == 0 and k % tk == 0, (name, m, n, k, tm, tn, tk)
    nk = k // tk
    whole_k_blocks = b3 and tb and not ta and nk == 1 and b.shape[0] > 1
    n_extra = len(extra)
    n_out = len(out_dtypes)
    grid = (m // tm, n // tn, nk)
    ex_in, ex_in_specs, ex_out_specs, ex_out_shape, ex_scratch = _hosted(exchange)

    def body(*refs):
        a_ref, b_ref = refs[0], refs[1]
        extra_refs = refs[2:2 + n_extra]
        first_out = 2 + n_extra + len(ex_in)
        out_refs = refs[first_out:first_out + n_out]
        ex_refs = refs[2 + n_extra:first_out] + refs[first_out + n_out:first_out + n_out + len(ex_out_shape)] + refs[-2:]
        step = [pl.program_id(d) for d in range(3)]

        if exchange is not None:
            @pl.when((step[0] == 0) & (step[1] == 0) & (step[2] == 0))
            def _():
                exchange.start(*exchange.split(ex_refs))

        def finish(acc):
            outs = (acc,) if epilogue is None else epilogue(acc, *[r[...] for r in extra_refs])
            for o_ref, val in zip(out_refs, outs):
                o_ref[...] = val.astype(o_ref.dtype)

        if whole_k_blocks:
            width = b.shape[2]
            part = _dot(a_ref[:, 0:width], b_ref[0], 1, 1)
            for blk in range(1, b.shape[0]):
                part = part + _dot(a_ref[:, blk * width:(blk + 1) * width], b_ref[blk], 1, 1)
        else:
            part = _dot(a_ref[...], b_ref[...], 0 if ta else 1, 1 if tb else 0)
        if nk == 1:
            finish(part)
        else:
            acc_ref = refs[first_out + n_out + len(ex_out_shape)]

            @pl.when(step[2] == 0)
            def _():
                acc_ref[...] = part

            @pl.when(step[2] > 0)
            def _():
                acc_ref[...] += part

            @pl.when(step[2] == nk - 1)
            def _():
                finish(acc_ref[...])

        if exchange is not None:
            @pl.when((step[0] == grid[0] - 1) & (step[1] == grid[1] - 1) & (step[2] == nk - 1))
            def _():
                exchange.finish(*exchange.split(ex_refs))

    a_spec = pl.BlockSpec((tk, tm), lambda i, j, kk: (kk, i)) if ta else pl.BlockSpec((tm, tk), lambda i, j, kk: (i, kk))
    if whole_k_blocks:
        b_spec = pl.BlockSpec((b.shape[0], tn, b.shape[2]), lambda i, j, kk: (0, j, 0))
    elif b3 and tb:
        assert b.shape[2] == tk
        b_spec = pl.BlockSpec((None, tn, tk), lambda i, j, kk: (kk, j, 0))
    elif b3:
        assert b.shape[2] == tn
        b_spec = pl.BlockSpec((None, tk, tn), lambda i, j, kk: (j, kk, 0))
    elif tb:
        b_spec = pl.BlockSpec((tn, tk), lambda i, j, kk: (j, kk))
    else:
        b_spec = pl.BlockSpec((tk, tn), lambda i, j, kk: (kk, j))
    tile = pl.BlockSpec((tm, tn), lambda i, j, kk: (i, j))
    out_specs = [tile] * n_out
    out_shape = [jax.ShapeDtypeStruct((m, n), dt) for dt in out_dtypes]
    if o3:
        out_specs[0] = pl.BlockSpec((None, tm, tn), lambda i, j, kk: (j, i, 0))
        out_shape[0] = jax.ShapeDtypeStruct((n // tn, m, tn), out_dtypes[0])
    res = pl.pallas_call(
        body, name=name, grid=grid,
        in_specs=[a_spec, b_spec] + [tile] * n_extra + ex_in_specs, out_specs=out_specs + ex_out_specs,
        out_shape=out_shape + ex_out_shape,
        scratch_shapes=([pltpu.VMEM((tm, tn), F32)] if nk > 1 else []) + ex_scratch,
        compiler_params=_cparams(),
    )(a, b, *extra, *ex_in)
    if exchange is not None:
        return (res[0] if n_out == 1 else res[:n_out]), res[n_out:]
    return res[0] if n_out == 1 else res


TR = 256


def _row_spec(cols):
    return pl.BlockSpec((TR, cols), lambda i: (i, 0))


def _vec_spec(cols):
    return pl.BlockSpec((1, cols), lambda i: (0, 0))


def _pre_norm(x, w, exchange=None):
    ex_in, ex_in_specs, ex_out_specs, ex_out_shape, ex_scratch = _hosted(exchange)

    def body(*refs):
        x_ref, w_ref, h_ref = refs[0], refs[1], refs[2 + len(ex_in)]
        ex_refs = refs[2:2 + len(ex_in)] + refs[3 + len(ex_in):]
        if exchange is not None:
            @pl.when(pl.program_id(0) == 0)
            def _():
                exchange.start(*exchange.split(ex_refs))

        xv = x_ref[...]
        h_ref[...] = (xv * _rms_scale(xv) * w_ref[...]).astype(BF16)

        if exchange is not None:
            @pl.when(pl.program_id(0) == S // TR - 1)
            def _():
                exchange.finish(*exchange.split(ex_refs))

    res = pl.pallas_call(
        body, name="pre_norm", grid=(S // TR,), in_specs=[_row_spec(D), _vec_spec(D)] + ex_in_specs,
        out_specs=[_row_spec(D)] + ex_out_specs, out_shape=[jax.ShapeDtypeStruct((S, D), BF16)] + ex_out_shape,
        scratch_shapes=ex_scratch, compiler_params=_cparams(),
    )(x, w, *ex_in)
    return res[0], res[1:]


def _post_mix(x, mixed, w_post, w_pre_mlp):
    def body(x_ref, m_ref, wp_ref, wm_ref, x1_ref, h2_ref):
        mv = m_ref[...]
        x1 = x_ref[...] + mv * _rms_scale(mv) * wp_ref[...]
        x1_ref[...] = x1
        h2_ref[...] = (x1 * _rms_scale(x1) * wm_ref[...]).astype(BF16)

    return pl.pallas_call(
        body, name="post_mix", grid=(S // TR,),
        in_specs=[_row_spec(D), _row_spec(D), _vec_spec(D), _vec_spec(D)], out_specs=[_row_spec(D), _row_spec(D)],
        out_shape=[jax.ShapeDtypeStruct((S, D), F32), jax.ShapeDtypeStruct((S, D), BF16)], compiler_params=_cparams(),
    )(x, mixed, w_post, w_pre_mlp)


def _loss_head(x1, y, w_post_mlp, target):
    def body(x1_ref, y_ref, w_ref, t_ref, dx2_ref, dy_ref, dw_ref, loss_ref):
        i = pl.program_id(0)
        yv = y_ref[...]
        w = w_ref[...]
        x2 = x1_ref[...] + yv * _rms_scale(yv) * w
        err = x2 - t_ref[...]
        dx2 = err * (1.0 / D)
        dx2_ref[...] = dx2
        dy, dwt = _rms_bwd(yv, w, dx2)
        dy_ref[...] = dy.astype(BF16)

        @pl.when(i == 0)
        def _():
            dw_ref[...] = jnp.zeros_like(dw_ref)
            loss_ref[...] = jnp.zeros_like(loss_ref)

        dw_ref[...] += jnp.sum(dwt, axis=0, keepdims=True)
        part = 0.5 * jnp.sum(jnp.mean(err * err, axis=-1, keepdims=True), axis=0, keepdims=True)
        loss_ref[...] += jnp.broadcast_to(part, loss_ref.shape)

    return pl.pallas_call(
        body, name="loss_head", grid=(S // TR,),
        in_specs=[_row_spec(D), _row_spec(D), _vec_spec(D), _row_spec(D)],
        out_specs=[_row_spec(D), _row_spec(D), _vec_spec(D), _vec_spec(LANES)],
        out_shape=[jax.ShapeDtypeStruct((S, D), F32), jax.ShapeDtypeStruct((S, D), BF16),
                   jax.ShapeDtypeStruct((1, D), F32), jax.ShapeDtypeStruct((1, LANES), F32)],
        compiler_params=_cparams(),
    )(x1, y, w_post_mlp, target)


def _mid_bwd(dh2, x1, w_pre_mlp, dx2, mixed, w_post):
    def body(dh2_ref, x1_ref, wm_ref, dx2_ref, m_ref, wp_ref, dx1_ref, dm_ref, dwm_ref, dwp_ref):
        i = pl.program_id(0)
        dxa, dwm = _rms_bwd(x1_ref[...], wm_ref[...], dh2_ref[...])
        dx1 = dx2_ref[...] + dxa
        dx1_ref[...] = dx1
        dm, dwp = _rms_bwd(m_ref[...], wp_ref[...], dx1)
        dm_ref[...] = dm.astype(BF16)

        @pl.when(i == 0)
        def _():
            dwm_ref[...] = jnp.zeros_like(dwm_ref)
            dwp_ref[...] = jnp.zeros_like(dwp_ref)

        dwm_ref[...] += jnp.sum(dwm, axis=0, keepdims=True)
        dwp_ref[...] += jnp.sum(dwp, axis=0, keepdims=True)

    return pl.pallas_call(
        body, name="mid_bwd", grid=(S // TR,),
        in_specs=[_row_spec(D), _row_spec(D), _vec_spec(D), _row_spec(D), _row_spec(D), _vec_spec(D)],
        out_specs=[_row_spec(D), _row_spec(D), _vec_spec(D), _vec_spec(D)],
        out_shape=[jax.ShapeDtypeStruct((S, D), F32), jax.ShapeDtypeStruct((S, D), BF16),
                   jax.ShapeDtypeStruct((1, D), F32), jax.ShapeDtypeStruct((1, D), F32)],
        compiler_params=_cparams(),
    )(dh2, x1, w_pre_mlp, dx2, mixed, w_post)


def _pre_norm_bwd(dh, x, w, dx1):
    def body(dh_ref, x_ref, w_ref, dx1_ref, dx_ref, dw_ref):
        i = pl.program_id(0)
        dxa, dwt = _rms_bwd(x_ref[...], w_ref[...], dh_ref[...])
        dx_ref[...] = dx1_ref[...] + dxa

        @pl.when(i == 0)
        def _():
            dw_ref[...] = jnp.zeros_like(dw_ref)

        dw_ref[...] += jnp.sum(dwt, axis=0, keepdims=True)

    return pl.pallas_call(
        body, name="pre_norm_bwd", grid=(S // TR,),
        in_specs=[_row_spec(D), _row_spec(D), _vec_spec(D), _row_spec(D)], out_specs=[_row_spec(D), _vec_spec(D)],
        out_shape=[jax.ShapeDtypeStruct((S, D), F32), jax.ShapeDtypeStruct((1, D), F32)], compiler_params=_cparams(),
    )(dh, x, w, dx1)


BQ = 256
NQ = S // BQ
LANE_BETA, LANE_G = 8, 12


def _gate_lanes(shape):
    lane = lax.broadcasted_iota(jnp.int32, shape, 1)
    return lane < LANE_BETA, (lane >= LANE_BETA) & (lane < LANE_G), (lane >= LANE_G) & (lane < LANE_G + NGH)


def _gates(proj, bias_vec, alog_vec):
    def body(s_ref, b_ref, a_ref, o_ref, carry_ref):
        i = pl.program_id(0)

        @pl.when(i == 0)
        def _():
            carry_ref[...] = jnp.zeros_like(carry_ref)

        z = s_ref[...] + b_ref[...]
        tail = jnp.log(1.0 + jnp.exp(-jnp.abs(z)))
        sp = jnp.maximum(z, 0.0) + tail
        lf = jnp.minimum(z, 0.0) - tail
        r = lax.broadcasted_iota(jnp.int32, (BQ, BQ), 0)
        c = lax.broadcasted_iota(jnp.int32, (BQ, BQ), 1)
        tri = (c <= r).astype(F32)
        cum = _hdot(tri, lf) + carry_ref[...]
        carry_ref[...] = cum[BQ - 1:BQ, :]
        is_fox, is_beta, is_g = _gate_lanes(z.shape)
        o_ref[...] = jnp.where(is_fox, cum, jnp.where(is_beta, _sigmoid(z), jnp.where(is_g, -jnp.exp(a_ref[...]) * sp, 0.0)))

    return pl.pallas_call(
        body, name="gates", grid=(NQ,),
        in_specs=[pl.BlockSpec((BQ, LANES), lambda i: (i, BLK_SMALL)), _vec_spec(LANES), _vec_spec(LANES)],
        out_specs=pl.BlockSpec((BQ, LANES), lambda i: (i, 0)), out_shape=jax.ShapeDtypeStruct((S, LANES), F32),
        scratch_shapes=[pltpu.VMEM((1, LANES), F32)], compiler_params=_cparams(),
    )(proj, bias_vec, alog_vec)


def _gates_bwd(proj, bias_vec, alog_vec, dgates_gdn, dcum_fox, dproj):
    def body(s_ref, b_ref, a_ref, dg_ref, dc_ref, dproj_in, dproj_ref, red_ref, carry_ref):
        del dproj_in
        i = pl.program_id(0)

        @pl.when(i == 0)
        def _():
            carry_ref[...] = jnp.zeros_like(carry_ref)
            red_ref[...] = jnp.zeros_like(red_ref)

        z = s_ref[...] + b_ref[...]
        dg = dg_ref[...] + dc_ref[...]
        r = lax.broadcasted_iota(jnp.int32, (BQ, BQ), 0)
        c = lax.broadcasted_iota(jnp.int32, (BQ, BQ), 1)
        upper = (c >= r).astype(F32)
        dlf = _hdot(upper, dg) + carry_ref[...]
        carry_ref[...] = dlf[0:1, :]
        sig = _sigmoid(z)
        g_scale = -jnp.exp(a_ref[...])
        is_fox, is_beta, is_g = _gate_lanes(z.shape)
        ds = jnp.where(is_fox, dlf * (1.0 - sig), jnp.where(is_beta, dg * sig * (1.0 - sig), jnp.where(is_g, dg * g_scale * sig, 0.0)))
        dproj_ref[:, 0:LANES] = ds.astype(BF16)
        dproj_ref[:, LANES:2 * LANES] = jnp.zeros((BQ, LANES), BF16)
        dalog = jnp.where(is_g, dg * g_scale * _softplus(z), 0.0)
        sums = jnp.sum(ds, axis=0, keepdims=True)
        red_ref[0:1, :] += jnp.where(is_fox[0:1], sums, 0.0)
        red_ref[1:2, :] += pltpu.roll(jnp.where(is_g[0:1], sums, 0.0), LANES - LANE_G, 1)
        red_ref[2:3, :] += pltpu.roll(jnp.sum(dalog, axis=0, keepdims=True), LANES - LANE_G, 1)

    blk = pl.BlockSpec((BQ, LANES), lambda i: (NQ - 1 - i, 0))
    return pl.pallas_call(
        body, name="gates_bwd", grid=(NQ,),
        in_specs=[pl.BlockSpec((BQ, LANES), lambda i: (NQ - 1 - i, BLK_SMALL)), _vec_spec(LANES), _vec_spec(LANES), blk, blk,
                  pl.BlockSpec(memory_space=pl.ANY)],
        out_specs=[pl.BlockSpec((BQ, 2 * LANES), lambda i: (NQ - 1 - i, BLK_SMALL // 2)), pl.BlockSpec((8, LANES), lambda i: (0, 0))],
        out_shape=[jax.ShapeDtypeStruct((S, DPROJ_PAD), BF16), jax.ShapeDtypeStruct((8, LANES), F32)],
        input_output_aliases={5: 0},
        scratch_shapes=[pltpu.VMEM((1, LANES), F32)], compiler_params=_cparams(),
    )(proj, bias_vec, alog_vec, dgates_gdn, dcum_fox, dproj)


FOX_SCALE = FHD ** -0.5
FOX_PAIRS = 2
FOX_PAIRS_BWD = 2


def _head_mask(e):
    lane = lax.broadcasted_iota(jnp.int32, (1, LANES), 1)
    return (lane >= e * FHD) & (lane < (e + 1) * FHD)


def _lane_col(vals, index):
    lane = lax.broadcasted_iota(jnp.int32, vals.shape, 1)
    return jnp.sum(jnp.where(lane == index, vals, 0.0), axis=1, keepdims=True)


def _sublane_row(vals, index):
    row = lax.broadcasted_iota(jnp.int32, vals.shape, 0)
    return jnp.sum(jnp.where(row == index, vals, 0.0), axis=0, keepdims=True)


def _pair_cols(c0, c1):
    lane = lax.broadcasted_iota(jnp.int32, (c0.shape[0], 2), 1)
    return jnp.where(lane == 0, c0, c1)


def _split3(x):
    hi = x.astype(BF16).astype(F32)
    rest = x - hi
    mid = rest.astype(BF16).astype(F32)
    return hi, mid, (rest - mid).astype(BF16).astype(F32)


def _fox_operand(vals, e, cum, is_query):
    lane = lax.broadcasted_iota(jnp.int32, (1, LANES), 1)
    base = (1 - e) * FHD
    parts = _split3(cum)
    own = jnp.where(_head_mask(e), vals * FOX_SCALE if is_query else vals, 0.0)
    cum_at, ones_at = (base, base + 3) if is_query else (base + 3, base)
    sign = 1.0 if is_query else -1.0
    out = own + jnp.where((lane >= ones_at) & (lane < ones_at + 3), 1.0, 0.0)
    for i, part in enumerate(parts):
        out = out + jnp.where(lane == cum_at + i, sign * part, 0.0)
    return out.astype(BF16)


def _causal_block():
    return lax.broadcasted_iota(jnp.int32, (BQ, BQ), 1) <= lax.broadcasted_iota(jnp.int32, (BQ, BQ), 0)


def _head_rms(o, masks):
    o2 = o * o
    r = [lax.rsqrt(jnp.sum(jnp.where(mk, o2, 0.0), axis=1, keepdims=True) * (1.0 / FHD) + EPS) for mk in masks]
    return jnp.where(masks[0], r[0], r[1])


def _hosted(exchange):
    if exchange is None:
        return [], [], [], [], []
    return (exchange.inputs, [HBM] * len(exchange.inputs), [HBM] * len(exchange.out_shape), exchange.out_shape,
            exchange.sem_shapes())


def _fox_fwd(proj, gates, w2, exchange=None):
    ex_in, ex_in_specs, ex_out_specs, ex_out_shape, ex_scratch = _hosted(exchange)

    n_in = 3 * FOX_PAIRS + 2
    heads = [(pp, e) for pp in range(FOX_PAIRS) for e in range(2)]

    def body(*refs):
        qkv_refs, g_ref, w_ref = refs[:3 * FOX_PAIRS], refs[3 * FOX_PAIRS], refs[3 * FOX_PAIRS + 1]
        mix_ref, o_ref, lse_ref = refs[n_in + len(ex_in):n_in + 3 + len(ex_in)]
        ka_ref, vb_ref = refs[n_in + 3 + len(ex_in) + len(ex_out_shape):n_in + 5 + len(ex_in) + len(ex_out_shape)]
        ex_refs = refs[n_in:n_in + len(ex_in)] + refs[n_in + 3 + len(ex_in):n_in + 3 + len(ex_in) + len(ex_out_shape)] + refs[-2:]
        grp, qi = pl.program_id(0), pl.program_id(1)

        def head_index(pp, e):
            return 2 * (FOX_PAIRS * grp + pp) + e

        if exchange is not None:
            @pl.when((grp == 0) & (qi == 0))
            def _():
                exchange.start(*exchange.split(ex_refs))

        @pl.when(qi == 0)
        def _():
            gt = g_ref[...]
            for pp in range(FOX_PAIRS):
                kv = qkv_refs[3 * pp + 1][...]
                for e in range(2):
                    ka_ref[2 * pp + e] = _fox_operand(kv, e, _lane_col(gt, head_index(pp, e)), False)
                vb_ref[pp] = qkv_refs[3 * pp + 2][...].astype(BF16)

        masks = [_head_mask(0), _head_mask(1)]
        gt = g_ref[pl.ds(pl.multiple_of(qi * BQ, BQ), BQ), :]
        qs = [_fox_operand(qkv_refs[3 * pp][...], e, _lane_col(gt, head_index(pp, e)), True) for pp, e in heads]
        n = range(len(heads))

        def block(kj, carry, diagonal):
            rows = pl.ds(pl.multiple_of(kj * BQ, BQ), BQ)
            s = [_dot(qs[i], ka_ref[i, rows, :], 1, 1) for i in n]
            if diagonal:
                s = [jnp.where(_causal_block(), s[i], -jnp.inf) for i in n]
            m_new = [jnp.maximum(carry[i][0], jnp.max(s[i], axis=-1, keepdims=True)) for i in n]
            p = [jnp.exp(s[i] - m_new[i]) for i in n]
            alpha = [jnp.exp(carry[i][0] - m_new[i]) for i in n]
            l_new = [alpha[i] * carry[i][1] + jnp.sum(p[i], axis=-1, keepdims=True) for i in n]
            pv = [_dot(p[i], vb_ref[heads[i][0], rows, :]) for i in n]
            return tuple((m_new[i], l_new[i], alpha[i] * carry[i][2] + pv[i]) for i in n)

        one = (jnp.full((BQ, 1), -jnp.inf, F32), jnp.zeros((BQ, 1), F32), jnp.zeros((BQ, LANES), F32))
        below = lax.fori_loop(0, qi, lambda kj, carry: block(kj, carry, False), (one,) * len(heads))
        done = block(qi, below, True)
        for pp in range(FOX_PAIRS):
            (m0, l0, a0), (m1, l1, a1) = done[2 * pp], done[2 * pp + 1]
            o = jnp.where(masks[0], a0 / l0, a1 / l1)
            cols = slice(pp * LANES, (pp + 1) * LANES)
            o_ref[:, cols] = o
            mix_ref[:, cols] = (o * _head_rms(o, masks) * w_ref[...]).astype(BF16)
            lse_ref[pp] = _pair_cols(m0 + jnp.log(l0), m1 + jnp.log(l1))

        if exchange is not None:
            @pl.when((grp == NPAIR // FOX_PAIRS - 1) & (qi == NQ - 1))
            def _():
                exchange.finish(*exchange.split(ex_refs))

    qkv_specs = []
    for pp in range(FOX_PAIRS):
        qkv_specs.append(pl.BlockSpec((BQ, LANES), lambda g, i, pp=pp: (i, 3 * (FOX_PAIRS * g + pp))))
        qkv_specs.append(pl.BlockSpec((S, LANES), lambda g, i, pp=pp: (0, 3 * (FOX_PAIRS * g + pp) + 1)))
        qkv_specs.append(pl.BlockSpec((S, LANES), lambda g, i, pp=pp: (0, 3 * (FOX_PAIRS * g + pp) + 2)))
    blk = pl.BlockSpec((BQ, FOX_PAIRS * LANES), lambda g, i: (i, g))
    res = pl.pallas_call(
        body, name="fox_fwd", grid=(NPAIR // FOX_PAIRS, NQ),
        in_specs=qkv_specs + [pl.BlockSpec((S, LANES), lambda g, i: (0, 0)), pl.BlockSpec((1, LANES), lambda g, i: (0, 0))]
        + ex_in_specs,
        out_specs=[blk, blk, pl.BlockSpec((FOX_PAIRS, BQ, 2), lambda g, i: (g, i, 0))] + ex_out_specs,
        out_shape=[jax.ShapeDtypeStruct((S, D), BF16), jax.ShapeDtypeStruct((S, DFOX), F32),
                   jax.ShapeDtypeStruct((NPAIR, S, 2), F32)] + ex_out_shape,
        scratch_shapes=[pltpu.VMEM((2 * FOX_PAIRS, S, LANES), BF16), pltpu.VMEM((FOX_PAIRS, S, LANES), BF16)] + ex_scratch,
        compiler_params=_cparams(),
    )(*([proj] * (3 * FOX_PAIRS)), gates, w2, *ex_in)
    return res[0], res[1], res[2], res[3:]


def _fox_norm_bwd(o, dmix, w2, exchange=None):
    ex_in, ex_in_specs, ex_out_specs, ex_out_shape, ex_scratch = _hosted(exchange)

    def body(*refs):
        o_ref, g_ref, w_ref = refs[:3]
        do_ref, dl_ref, dw_ref = refs[3 + len(ex_in):6 + len(ex_in)]
        ex_refs = refs[3:3 + len(ex_in)] + refs[6 + len(ex_in):]
        hp, qi = pl.program_id(0), pl.program_id(1)

        if exchange is not None:
            @pl.when((hp == 0) & (qi == 0))
            def _():
                exchange.start(*exchange.split(ex_refs))

        masks = [_head_mask(0), _head_mask(1)]
        ov = o_ref[...]
        g = g_ref[...]
        r = _head_rms(ov, masks)
        gw = g * w_ref[...]
        gwo = gw * ov
        mean = [jnp.sum(jnp.where(mk, gwo, 0.0), axis=1, keepdims=True) * (1.0 / FHD) for mk in masks]
        do = r * gw - ov * (r * r * r) * jnp.where(masks[0], mean[0], mean[1])
        do_ref[...] = do.astype(BF16)
        doo = do * ov
        dl_ref[...] = _pair_cols(*[jnp.sum(jnp.where(mk, doo, 0.0), axis=1, keepdims=True) for mk in masks])

        @pl.when((hp == 0) & (qi == 0))
        def _():
            dw_ref[...] = jnp.zeros_like(dw_ref)

        dw_ref[...] += jnp.sum(g * ov * r, axis=0, keepdims=True)

        @pl.when((hp == NPAIR - 1) & (qi == NQ - 1))
        def _():
            dw = dw_ref[...]
            dw_ref[...] = dw + pltpu.roll(dw, FHD, 1)
            if exchange is not None:
                exchange.finish(*exchange.split(ex_refs))

    blk = pl.BlockSpec((BQ, LANES), lambda hp, i: (i, hp))
    vec = pl.BlockSpec((1, LANES), lambda hp, i: (0, 0))
    res = pl.pallas_call(
        body, name="fox_norm_bwd", grid=(NPAIR, NQ), in_specs=[blk, blk, vec] + ex_in_specs,
        out_specs=[blk, pl.BlockSpec((None, BQ, 2), lambda hp, i: (hp, i, 0)), vec] + ex_out_specs,
        out_shape=[jax.ShapeDtypeStruct((S, DFOX), BF16), jax.ShapeDtypeStruct((NPAIR, S, 2), F32),
                   jax.ShapeDtypeStruct((1, LANES), F32)] + ex_out_shape,
        scratch_shapes=ex_scratch, compiler_params=_cparams(),
    )(o, dmix, w2, *ex_in)
    return res[0], res[1], res[2], res[3:]


def _fox_bwd(proj, do, gates, lse, delta, exchange=None):
    ex_in, ex_in_specs, ex_out_specs, ex_out_shape, ex_scratch = _hosted(exchange)

    pg = FOX_PAIRS_BWD
    n_in = 3 * pg + 4
    heads = [(pp, e) for pp in range(pg) for e in range(2)]

    def body(*refs):
        qkv_refs = refs[:3 * pg]
        do_ref, g_ref, lse_ref, dl_ref = refs[3 * pg:n_in]
        dproj_ref, dc_ref = refs[n_in + len(ex_in):n_in + 2 + len(ex_in)]
        qa_ref, dq_ref = refs[n_in + 2 + len(ex_in) + len(ex_out_shape):n_in + 4 + len(ex_in) + len(ex_out_shape)]
        ex_refs = refs[n_in:n_in + len(ex_in)] + refs[n_in + 2 + len(ex_in):n_in + 2 + len(ex_in) + len(ex_out_shape)] + refs[-2:]
        grp, kj = pl.program_id(0), pl.program_id(1)

        def head_index(pp, e):
            return 2 * (pg * grp + pp) + e

        if exchange is not None:
            @pl.when((grp == 0) & (kj == 0))
            def _():
                exchange.start(*exchange.split(ex_refs))

        @pl.when(kj == 0)
        def _():
            gt = g_ref[...]
            for pp in range(pg):
                qv = qkv_refs[3 * pp][...]
                for e in range(2):
                    qa_ref[2 * pp + e] = _fox_operand(qv, e, _lane_col(gt, head_index(pp, e)), True)
            dq_ref[...] = jnp.zeros_like(dq_ref)

        @pl.when((grp == 0) & (kj == 0))
        def _():
            dc_ref[...] = jnp.zeros_like(dc_ref)

        masks = [_head_mask(0), _head_mask(1)]
        krows = pl.ds(pl.multiple_of(kj * BQ, BQ), BQ)
        gk = g_ref[krows, :]
        kas = [_fox_operand(qkv_refs[3 * pp + 1][...], e, _lane_col(gk, head_index(pp, e)), False) for pp, e in heads]
        vbs = [qkv_refs[3 * pp + 2][...].astype(BF16) for pp in range(pg)]
        lane = lax.broadcasted_iota(jnp.int32, (BQ, LANES), 1)
        n = range(len(heads))

        def block(qi, carry, diagonal):
            dks, dvs, css = carry
            rows = pl.ds(pl.multiple_of(qi * BQ, BQ), BQ)
            qa = [qa_ref[i, rows, :] for i in n]
            s = [_dot(qa[i], kas[i], 1, 1) for i in n]
            if diagonal:
                s = [jnp.where(_causal_block(), s[i], -jnp.inf) for i in n]
            dov = [do_ref[rows, pp * LANES:(pp + 1) * LANES] for pp in range(pg)]
            doe = [jnp.where(masks[e], dov[pp], jnp.zeros_like(dov[pp])) for pp, e in heads]
            lse2 = [lse_ref[pp, rows, :] for pp in range(pg)]
            dl2 = [dl_ref[pp, rows, :] for pp in range(pg)]
            p = [jnp.exp(s[i] - _lane_col(lse2[heads[i][0]], heads[i][1])) for i in n]
            dp = [_dot(doe[i], vbs[heads[i][0]], 1, 1) for i in n]
            ds = [p[i] * (dp[i] - _lane_col(dl2[heads[i][0]], heads[i][1])) for i in n]
            dv_part = [_dot(p[i], doe[i], 0, 0) for i in n]
            dk_part = [_dot(ds[i], jnp.where(masks[heads[i][1]], qa[i], jnp.zeros_like(qa[i])), 0, 0) for i in n]
            dq_part = [jnp.where(masks[heads[i][1]], _dot(ds[i], kas[i]), 0.0) for i in n]
            css = tuple(css[i] + jnp.sum(ds[i], axis=0, keepdims=True) for i in n)
            dc = jnp.zeros((BQ, LANES), F32)
            for i in n:
                dc = dc + jnp.where(lane == head_index(*heads[i]), jnp.sum(ds[i], axis=1, keepdims=True), 0.0)
            for pp in range(pg):
                dq_ref[pp, rows, :] += (dq_part[2 * pp] + dq_part[2 * pp + 1]) * FOX_SCALE
            dc_ref[rows, :] += dc
            dks = tuple(dks[pp] + dk_part[2 * pp] + dk_part[2 * pp + 1] for pp in range(pg))
            dvs = tuple(dvs[pp] + dv_part[2 * pp] + dv_part[2 * pp + 1] for pp in range(pg))
            return dks, dvs, css

        zero = jnp.zeros((BQ, LANES), F32)
        first = block(kj, ((zero,) * pg, (zero,) * pg, (jnp.zeros((1, BQ), F32),) * len(heads)), True)
        dks, dvs, css = lax.fori_loop(kj + 1, NQ, lambda qi, carry: block(qi, carry, False), first)
        r = lax.broadcasted_iota(jnp.int32, (BQ, BQ), 0)
        c = lax.broadcasted_iota(jnp.int32, (BQ, BQ), 1)
        dcol = jnp.zeros((BQ, LANES), F32)
        for i in n:
            col = jnp.sum(jnp.where(r == c, css[i], 0.0), axis=1, keepdims=True)
            dcol = dcol + jnp.where(lane == head_index(*heads[i]), col, 0.0)
        dc_ref[krows, :] -= dcol
        for pp in range(pg):
            base = 3 * pp * LANES
            dproj_ref[krows, base + LANES:base + 2 * LANES] = dks[pp].astype(BF16)
            dproj_ref[krows, base + 2 * LANES:base + 3 * LANES] = dvs[pp].astype(BF16)

        @pl.when(kj == NQ - 1)
        def _():
            for pp in range(pg):
                dproj_ref[:, 3 * pp * LANES:(3 * pp + 1) * LANES] = dq_ref[pp].astype(BF16)

        if exchange is not None:
            @pl.when((grp == NPAIR // pg - 1) & (kj == NQ - 1))
            def _():
                exchange.finish(*exchange.split(ex_refs))

    qkv_specs = []
    for pp in range(pg):
        qkv_specs.append(pl.BlockSpec((S, LANES), lambda g, j, pp=pp: (0, 3 * (pg * g + pp))))
        qkv_specs.append(pl.BlockSpec((BQ, LANES), lambda g, j, pp=pp: (j, 3 * (pg * g + pp) + 1)))
        qkv_specs.append(pl.BlockSpec((BQ, LANES), lambda g, j, pp=pp: (j, 3 * (pg * g + pp) + 2)))
    pair = pl.BlockSpec((pg, S, 2), lambda g, j: (g, 0, 0))
    res = pl.pallas_call(
        body, name="fox_bwd", grid=(NPAIR // pg, NQ),
        in_specs=qkv_specs + [pl.BlockSpec((S, pg * LANES), lambda g, j: (0, g)), pl.BlockSpec((S, LANES), lambda g, j: (0, 0)),
                              pair, pair] + ex_in_specs,
        out_specs=[pl.BlockSpec((S, 3 * pg * LANES), lambda g, j: (0, g)), pl.BlockSpec((S, LANES), lambda g, j: (0, 0))]
        + ex_out_specs,
        out_shape=[jax.ShapeDtypeStruct((S, DPROJ_PAD), BF16), jax.ShapeDtypeStruct((S, LANES), F32)] + ex_out_shape,
        scratch_shapes=[pltpu.VMEM((2 * pg, S, LANES), BF16), pltpu.VMEM((pg, S, LANES), F32)] + ex_scratch,
        compiler_params=_cparams(),
    )(*([proj] * (3 * pg)), do, gates, lse, delta, *ex_in)
    return res[0], res[1], res[2:]


NQKV = 3 * NGH
GDN_QSCALE = GHD ** -0.5


def _shift_down(x, s):
    if s == 0:
        return x
    row = lax.broadcasted_iota(jnp.int32, x.shape, 0)
    return jnp.where(row >= s, pltpu.roll(x, s, 0), 0.0)


def _shift_up(x, s):
    if s == 0:
        return x
    n = x.shape[0]
    row = lax.broadcasted_iota(jnp.int32, x.shape, 0)
    return jnp.where(row < n - s, pltpu.roll(x, n - s, 0), 0.0)


def _conv_pre(xv, wv):
    pre = xv * wv[CONV_K - 1:CONV_K, :]
    for j in range(CONV_K - 1):
        pre = pre + _shift_down(xv, CONV_K - 1 - j) * wv[j:j + 1, :]
    return pre


def _l2_factors(b):
    return b < 2 * NGH, jnp.where(b < NGH, GDN_QSCALE, 1.0)


def _gdn_pre(proj, conv_w):
    def body(x_ref, w_ref, o_ref):
        b = pl.program_id(0)
        c = _silu(_conv_pre(x_ref[...], w_ref[...]))
        normed, scale = _l2_factors(b)
        rs = lax.rsqrt(jnp.sum(c * c, axis=-1, keepdims=True) + EPS)
        o_ref[...] = c * jnp.where(normed, rs, 1.0) * scale

    return pl.pallas_call(
        body, name="gdn_pre", grid=(NQKV,),
        in_specs=[pl.BlockSpec((S, GHD), lambda b: (0, BLK_GDN + b)), pl.BlockSpec((CONV_K, GHD), lambda b: (0, b))],
        out_specs=pl.BlockSpec((S, GHD), lambda b: (0, b)),
        out_shape=jax.ShapeDtypeStruct((S, NQKV * GHD), F32), compiler_params=_cparams(),
    )(proj, conv_w)


def _gdn_pre_bwd(proj, conv_w, dqkv, dproj):
    def body(x_ref, w_ref, dy_ref, dproj_in, dx_ref, dw_ref):
        del dproj_in
        b = pl.program_id(0)
        xv = x_ref[...]
        wv = w_ref[...]
        pre = _conv_pre(xv, wv)
        sig = _sigmoid(pre)
        c = pre * sig
        normed, scale = _l2_factors(b)
        g = dy_ref[...] * scale
        rs = lax.rsqrt(jnp.sum(c * c, axis=-1, keepdims=True) + EPS)
        dc_n = rs * g - c * (rs * rs * rs) * jnp.sum(g * c, axis=-1, keepdims=True)
        dc = jnp.where(normed, dc_n, g)
        dpre = dc * sig * (1.0 + pre * (1.0 - sig))
        dx = dpre * wv[CONV_K - 1:CONV_K, :]
        for j in range(CONV_K - 1):
            dx = dx + _shift_up(dpre, CONV_K - 1 - j) * wv[j:j + 1, :]
        dx_ref[...] = dx.astype(BF16)
        for j in range(CONV_K):
            dw_ref[j:j + 1, :] = jnp.sum(dpre * _shift_down(xv, CONV_K - 1 - j), axis=0, keepdims=True)

    return pl.pallas_call(
        body, name="gdn_pre_bwd", grid=(NQKV,),
        in_specs=[pl.BlockSpec((S, GHD), lambda b: (0, BLK_GDN + b)), pl.BlockSpec((CONV_K, GHD), lambda b: (0, b)),
                  pl.BlockSpec((None, S, GHD), lambda b: (b // NGH, 0, b % NGH)), pl.BlockSpec(memory_space=pl.ANY)],
        out_specs=[pl.BlockSpec((S, GHD), lambda b: (0, BLK_GDN + b)), pl.BlockSpec((CONV_K, GHD), lambda b: (0, b))],
        out_shape=[jax.ShapeDtypeStruct((S, DPROJ_PAD), BF16), jax.ShapeDtypeStruct((CONV_K, NQKV * GHD), F32)],
        input_output_aliases={3: 0}, compiler_params=_cparams(),
    )(proj, conv_w, dqkv, dproj)


CB = 16
NCB = NCH // CB


def _chunk_prep(qs, ks, vs, gcols, bcols, t_saved=None):
    n = range(len(qs))
    r = lax.broadcasted_iota(jnp.int32, (CHUNK, CHUNK), 0)
    c = lax.broadcasted_iota(jnp.int32, (CHUNK, CHUNK), 1)
    incl = c <= r
    eye = (r == c).astype(F32)
    grow = [jnp.sum(gcols[i] * eye, axis=0, keepdims=True) for i in n]
    gc_col = [jnp.sum(jnp.where(incl, grow[i], 0.0), axis=1, keepdims=True) for i in n]
    gc_row = [jnp.sum(jnp.where(r <= c, gcols[i], 0.0), axis=0, keepdims=True) for i in n]
    decay = [jnp.exp(jnp.where(incl, gc_col[i] - gc_row[i], -jnp.inf)) for i in n]
    kb = [ks[i] * bcols[i] for i in n]
    vb = [vs[i] * bcols[i] for i in n]
    kk = [_mm_nt(kb[i], ks[i]) for i in n]
    m = [jnp.where(c < r, kk[i] * decay[i], 0.0) for i in n]
    if t_saved is None:
        t_inv = [eye - m[i] for i in n]
        p = [_dot3(m[i], m[i]) for i in n]
        for step in range(5):
            t_inv = [t_inv[i] + _dot3(t_inv[i], p[i]) for i in n]
            if step < 4:
                p = [_dot3(p[i], p[i]) for i in n]
    else:
        t_inv = [_saved_inverse(m[i], t_saved[i]) for i in n]
    egc = [jnp.exp(gc_col[i]) for i in n]
    u = [_mm_nn(t_inv[i], vb[i]) for i in n]
    w = [_mm_nn(t_inv[i], kb[i] * egc[i]) for i in n]
    qk = [_mm_nt(qs[i], ks[i]) for i in n]
    gc_last = [gc_col[i][CHUNK - 1:CHUNK, :] for i in n]
    return [(u[i], w[i], qk[i] * decay[i], qs[i] * egc[i], ks[i] * jnp.exp(gc_last[i] - gc_col[i]), jnp.exp(gc_last[i]),
             t_inv[i]) for i in n]


def _prep_specs():
    rows = CB * CHUNK
    qs = pl.BlockSpec((rows, GHD), lambda i, h: (i, h))
    ks = pl.BlockSpec((rows, GHD), lambda i, h: (i, NGH + h))
    vs = pl.BlockSpec((rows, GHD), lambda i, h: (i, 2 * NGH + h))
    gs = pl.BlockSpec((rows, LANES), lambda i, h: (i, 0))
    a_s = pl.BlockSpec((None, rows, CHUNK), lambda i, h: (h, i, 0))
    gl_s = pl.BlockSpec((None, CB, 1, LANES), lambda i, h: (h, i, 0, 0))
    return qs, ks, vs, gs, a_s, gl_s


def _gdn_prep(qkv, gates, exchange=None):
    ex_in, ex_in_specs, ex_out_specs, ex_out_shape, ex_scratch = _hosted(exchange)

    def body(*refs):
        q_ref, k_ref, v_ref, g_ref = refs[:4]
        u_ref, w_ref, qd_ref, kd_ref, a_ref, gl_ref, t_ref = refs[4 + len(ex_in):11 + len(ex_in)]
        ex_refs = refs[4:4 + len(ex_in)] + refs[11 + len(ex_in):]
        h = pl.program_id(1)

        if exchange is not None:
            @pl.when((pl.program_id(0) == 0) & (h == 0))
            def _():
                exchange.start(*exchange.split(ex_refs))

        chunks = [pl.ds(cidx * CHUNK, CHUNK) for cidx in range(CB)]
        gts = [g_ref[rows, :] for rows in chunks]
        outs = _chunk_prep([q_ref[rows, :] for rows in chunks], [k_ref[rows, :] for rows in chunks],
                           [v_ref[rows, :] for rows in chunks], [_lane_col(gt, LANE_G + h) for gt in gts],
                           [_lane_col(gt, LANE_BETA + h) for gt in gts])
        for cidx, rows in enumerate(chunks):
            u, w, a, qd, kd, gl, t_inv = outs[cidx]
            u_ref[rows, :] = u
            w_ref[rows, :] = w
            qd_ref[rows, :] = qd
            kd_ref[rows, :] = kd
            a_ref[rows, :] = a
            t_ref[rows, :] = t_inv
            gl_ref[cidx] = jnp.broadcast_to(gl, (1, LANES))

        if exchange is not None:
            @pl.when((pl.program_id(0) == NCB - 1) & (h == NGH - 1))
            def _():
                exchange.finish(*exchange.split(ex_refs))

    qs, ks, vs, gs, a_s, gl_s = _prep_specs()
    tok = jax.ShapeDtypeStruct((S, DGDN), F32)
    sq = jax.ShapeDtypeStruct((NGH, S, CHUNK), F32)
    res = pl.pallas_call(
        body, name="gdn_prep", grid=(NCB, NGH), in_specs=[qs, ks, vs, gs] + ex_in_specs,
        out_specs=[qs, qs, qs, qs, a_s, gl_s, a_s] + ex_out_specs,
        out_shape=[tok, tok, tok, tok, sq, jax.ShapeDtypeStruct((NGH, NCH, 1, LANES), F32), sq] + ex_out_shape,
        scratch_shapes=ex_scratch, compiler_params=_cparams(),
    )(qkv, qkv, qkv, gates, *ex_in)
    return res[:7], res[7:]


def _gdn_prep_bwd(qkv, gates, t_inv, du, dw, dqd, dkd, da, dgl):
    def body(q_ref, k_ref, v_ref, g_ref, t_ref, du_ref, dw_ref, dqd_ref, dkd_ref, da_ref, dgl_ref, dqkv_ref, dg_ref):
        h = pl.program_id(1)

        @pl.when(h == 0)
        def _():
            dg_ref[...] = jnp.zeros_like(dg_ref)

        lane = lax.broadcasted_iota(jnp.int32, (CHUNK, LANES), 1)
        chunks = [pl.ds(cidx * CHUNK, CHUNK) for cidx in range(CB)]
        gts = [g_ref[rows, :] for rows in chunks]
        t_saved = [t_ref[rows, :] for rows in chunks]
        _, vjp = jax.vjp(lambda *args: [o[:6] for o in _chunk_prep(*args, t_saved=t_saved)],
                         [q_ref[rows, :] for rows in chunks], [k_ref[rows, :] for rows in chunks],
                         [v_ref[rows, :] for rows in chunks], [_lane_col(gt, LANE_G + h) for gt in gts],
                         [_lane_col(gt, LANE_BETA + h) for gt in gts])
        dqs, dks, dvs, dgcs, dbcs = vjp([(du_ref[rows, :], dw_ref[rows, :], da_ref[rows, :], dqd_ref[rows, :],
                                          dkd_ref[rows, :], dgl_ref[cidx][:, 0:1]) for cidx, rows in enumerate(chunks)])
        for cidx, rows in enumerate(chunks):
            dq, dk, dv, dgc, dbc = dqs[cidx], dks[cidx], dvs[cidx], dgcs[cidx], dbcs[cidx]
            dqkv_ref[0, rows, :] = dq
            dqkv_ref[1, rows, :] = dk
            dqkv_ref[2, rows, :] = dv
            dg_ref[rows, :] += jnp.where(lane == LANE_G + h, dgc, 0.0) + jnp.where(lane == LANE_BETA + h, dbc, 0.0)

    qs, ks, vs, gs, a_s, gl_s = _prep_specs()
    return pl.pallas_call(
        body, name="gdn_prep_bwd", grid=(NCB, NGH), in_specs=[qs, ks, vs, gs, a_s, qs, qs, qs, qs, a_s, gl_s],
        out_specs=[pl.BlockSpec((3, CB * CHUNK, GHD), lambda i, h: (0, i, h)), gs],
        out_shape=[jax.ShapeDtypeStruct((3, S, DGDN), F32), jax.ShapeDtypeStruct((S, LANES), F32)],
        compiler_params=_cparams(),
    )(qkv, qkv, qkv, gates, t_inv, du, dw, dqd, dkd, da, dgl)


def _scan_specs(nh, parts, reverse):
    wide, rows, chunks = nh * GHD, S // parts, NCH // parts

    def part(p):
        return parts - 1 - p if reverse else p

    hs = pl.BlockSpec((rows, wide), lambda g, p: (part(p), g))
    a_s = pl.BlockSpec((nh, rows, CHUNK), lambda g, p: (g, part(p), 0))
    gl_s = pl.BlockSpec((nh, chunks, 1, LANES), lambda g, p: (g, part(p), 0, 0))
    st_s = pl.BlockSpec((nh, chunks, GHD, GHD), lambda g, p: (g, part(p), 0, 0))
    gz_s = pl.BlockSpec((rows, wide), lambda g, p: (part(p), BLK_GZ // nh + g))
    mix_s = pl.BlockSpec((rows, wide), lambda g, p: (part(p), NPAIR // nh + g))
    return hs, a_s, gl_s, st_s, gz_s, mix_s


def _head_cols(hh):
    return slice(hh * GHD, (hh + 1) * GHD)


SCAN_HEADS, SCAN_PARTS = 4, 2
SCAN_HEADS_BWD, SCAN_PARTS_BWD = 2, 2


def _gdn_scan(u, w, qd, kd, a, gl, proj, w_norm, mix):
    heads = range(SCAN_HEADS)

    def body(u_ref, w_ref, qd_ref, kd_ref, a_ref, gl_ref, z_ref, wn_ref, mix_in, mix_ref, o_ref, st_ref, carry_ref):
        del mix_in

        @pl.when(pl.program_id(1) == 0)
        def _():
            carry_ref[...] = jnp.zeros_like(carry_ref)

        def step(ci, states):
            rows = pl.ds(pl.multiple_of(ci * CHUNK, CHUNK), CHUNK)
            for hh in heads:
                st_ref[hh, ci] = states[hh]
            ws = [_dot(w_ref[rows, _head_cols(hh)], states[hh]) for hh in heads]
            qs = [_dot(qd_ref[rows, _head_cols(hh)], states[hh]) for hh in heads]
            vn = [u_ref[rows, _head_cols(hh)] - ws[hh] for hh in heads]
            av = [_dot(a_ref[hh, rows, :], vn[hh]) for hh in heads]
            kv = [_dot(kd_ref[rows, _head_cols(hh)], vn[hh], 0, 0) for hh in heads]
            for hh in heads:
                o_ref[rows, _head_cols(hh)] = qs[hh] + av[hh]
            return tuple(states[hh] * gl_ref[hh, ci] + kv[hh] for hh in heads)

        last = lax.fori_loop(0, NCH // SCAN_PARTS, step, tuple(carry_ref[hh] for hh in heads))
        for hh in heads:
            carry_ref[hh] = last[hh]
            ov = o_ref[:, _head_cols(hh)]
            mix_ref[:, _head_cols(hh)] = (ov * _rms_scale(ov) * wn_ref[...] * _silu(z_ref[:, _head_cols(hh)])).astype(BF16)

    hs, a_s, gl_s, st_s, gz_s, mix_s = _scan_specs(SCAN_HEADS, SCAN_PARTS, False)
    return pl.pallas_call(
        body, name="gdn_scan", grid=(NGH // SCAN_HEADS, SCAN_PARTS),
        in_specs=[hs, hs, hs, hs, a_s, gl_s, gz_s, pl.BlockSpec((1, GHD), lambda g, p: (0, 0)),
                  pl.BlockSpec(memory_space=pl.ANY)],
        out_specs=[mix_s, hs, st_s],
        out_shape=[jax.ShapeDtypeStruct((S, D), BF16), jax.ShapeDtypeStruct((S, DGDN), F32),
                   jax.ShapeDtypeStruct((NGH, NCH, GHD, GHD), F32)],
        input_output_aliases={8: 0}, scratch_shapes=[pltpu.VMEM((SCAN_HEADS, GHD, GHD), F32)], compiler_params=_cparams(),
    )(u, w, qd, kd, a, gl, proj, w_norm, mix)


def _gdn_scan_bwd(dmix, o, proj, w_norm, u, w, qd, kd, a, gl, states, dproj):
    def body(dy_ref, o_ref, z_ref, wn_ref, u_ref, w_ref, qd_ref, kd_ref, a_ref, gl_ref, st_ref, dproj_in,
             dz_ref, du_ref, dw_ref, dqd_ref, dkd_ref, da_ref, dgl_ref, dwn_ref, do_ref, carry_ref):
        del dproj_in
        heads = range(SCAN_HEADS_BWD)
        chunks = NCH // SCAN_PARTS_BWD

        @pl.when((pl.program_id(0) == 0) & (pl.program_id(1) == 0))
        def _():
            dwn_ref[...] = jnp.zeros_like(dwn_ref)

        @pl.when(pl.program_id(1) == 0)
        def _():
            carry_ref[...] = jnp.zeros_like(carry_ref)

        wn = wn_ref[...]
        for hh in heads:
            c = _head_cols(hh)
            ov = o_ref[:, c]
            zv = z_ref[:, c]
            g = dy_ref[:, c]
            sig = _sigmoid(zv)
            dz_ref[:, c] = (g * (ov * _rms_scale(ov) * wn) * sig * (1.0 + zv * (1.0 - sig))).astype(BF16)
            do, dwt = _rms_bwd(ov, wn, g * zv * sig)
            do_ref[:, c] = do
            dwn_ref[...] += jnp.sum(dwt, axis=0, keepdims=True)

        def step(t, dstates):
            ci = chunks - 1 - t
            rows = pl.ds(pl.multiple_of(ci * CHUNK, CHUNK), CHUNK)
            cols = [_head_cols(hh) for hh in heads]
            state = [st_ref[hh, ci] for hh in heads]
            dov = [do_ref[rows, cols[hh]] for hh in heads]
            wv = [w_ref[rows, cols[hh]] for hh in heads]
            ws = [_dot(wv[hh], state[hh]) for hh in heads]
            adov = [_dot(a_ref[hh, rows, :], dov[hh], 0, 0) for hh in heads]
            kds = [_dot(kd_ref[rows, cols[hh]], dstates[hh]) for hh in heads]
            dqd = [_dot(dov[hh], state[hh], 1, 1) for hh in heads]
            qdo = [_dot(qd_ref[rows, cols[hh]], dov[hh], 0, 0) for hh in heads]
            vn = [u_ref[rows, cols[hh]] - ws[hh] for hh in heads]
            dvn = [adov[hh] + kds[hh] for hh in heads]
            da = [_dot(dov[hh], vn[hh], 1, 1) for hh in heads]
            dkd = [_dot(vn[hh], dstates[hh], 1, 1) for hh in heads]
            dwv = [_dot(dvn[hh], state[hh], 1, 1) for hh in heads]
            wdv = [_dot(wv[hh], dvn[hh], 0, 0) for hh in heads]
            for hh in heads:
                da_ref[hh, rows, :] = da[hh]
                dqd_ref[rows, cols[hh]] = dqd[hh]
                dkd_ref[rows, cols[hh]] = dkd[hh]
                dgl = jnp.sum(jnp.sum(dstates[hh] * state[hh], axis=1, keepdims=True), axis=0, keepdims=True)
                dgl_ref[hh, ci] = jnp.broadcast_to(dgl, (1, LANES))
                du_ref[rows, cols[hh]] = dvn[hh]
                dw_ref[rows, cols[hh]] = -dwv[hh]
            return tuple(dstates[hh] * gl_ref[hh, ci] + qdo[hh] - wdv[hh] for hh in heads)

        last = lax.fori_loop(0, chunks, step, tuple(carry_ref[hh] for hh in heads))
        for hh in heads:
            carry_ref[hh] = last[hh]

    hs, a_s, gl_s, st_s, gz_s, mix_s = _scan_specs(SCAN_HEADS_BWD, SCAN_PARTS_BWD, True)
    vec = pl.BlockSpec((1, GHD), lambda g, p: (0, 0))
    tok = jax.ShapeDtypeStruct((S, DGDN), F32)
    return pl.pallas_call(
        body, name="gdn_scan_bwd", grid=(NGH // SCAN_HEADS_BWD, SCAN_PARTS_BWD),
        in_specs=[mix_s, hs, gz_s, vec, hs, hs, hs, hs, a_s, gl_s, st_s, pl.BlockSpec(memory_space=pl.ANY)],
        out_specs=[gz_s, hs, hs, hs, hs, a_s, gl_s, vec],
        out_shape=[jax.ShapeDtypeStruct((S, DPROJ_PAD), BF16), tok, tok, tok, tok,
                   jax.ShapeDtypeStruct((NGH, S, CHUNK), F32), jax.ShapeDtypeStruct((NGH, NCH, 1, LANES), F32),
                   jax.ShapeDtypeStruct((1, GHD), F32)],
        input_output_aliases={11: 0},
        scratch_shapes=[pltpu.VMEM((S // SCAN_PARTS_BWD, SCAN_HEADS_BWD * GHD), F32), pltpu.VMEM((SCAN_HEADS_BWD, GHD, GHD), F32)],
        compiler_params=_cparams(),
    )(dmix, o, proj, w_norm, u, w, qd, kd, a, gl, states, dproj)


def _place():
    return lax.axis_index("x"), lax.axis_index("y"), lax.axis_index("c")


def _other_chips(x, y):
    return [(1 - x, y), (x, 1 - y), (1 - x, 1 - y)]


HBM = pl.BlockSpec(memory_space=pltpu.HBM)
VMEM = pl.BlockSpec(memory_space=pltpu.VMEM)


def _half_rows(ref_or_rows, half):
    rows = ref_or_rows // 2
    return pl.ds(pl.multiple_of(half * rows, rows), rows)


class _Exchange:
    def __init__(self, inputs, out_shape, n_sems, start, finish):
        self.inputs, self.out_shape, self.n_sems, self.start, self.finish = inputs, out_shape, n_sems, start, finish

    def sem_shapes(self):
        return [pltpu.SemaphoreType.DMA((self.n_sems,)), pltpu.SemaphoreType.DMA((self.n_sems,))]

    def split(self, refs):
        n_in, n_out = len(self.inputs), len(self.out_shape)
        return refs[:n_in], refs[n_in:n_in + n_out], refs[n_in + n_out], refs[n_in + n_out + 1]


def _run_exchange(ex, name):
    def body(*refs):
        parts = ex.split(refs)
        ex.start(*parts)
        ex.finish(*parts)

    return pl.pallas_call(
        body, name=name, in_specs=[HBM] * len(ex.inputs), out_specs=[HBM] * len(ex.out_shape), out_shape=ex.out_shape,
        scratch_shapes=ex.sem_shapes(), compiler_params=_cparams(),
    )(*ex.inputs)


def _allgather_exchange(shards, whole=()):
    n, nw = len(shards), len(whole)

    def plan(src, outs, send_sems, recv_sems):
        x, y, c = _place()
        chips = _other_chips(x, y)
        chip_ids = [2 * ch[0] + ch[1] for ch in chips]

        def copy(a, k, chip_index, half, to, from_src):
            rows = _half_rows(src[a].shape[0], half)
            dst = outs[a].at[chip_index, rows]
            return pltpu.make_async_remote_copy(
                src_ref=src[a].at[rows] if from_src else dst, dst_ref=dst, send_sem=send_sems.at[6 * a + k],
                recv_sem=recv_sems.at[6 * a + k], device_id=to, device_id_type=MESH)

        def whole_copy(b, k, chip_index, to):
            return pltpu.make_async_remote_copy(
                src_ref=src[n + b], dst_ref=outs[n + b].at[chip_index], send_sem=send_sems.at[6 * n + 3 * b + k],
                recv_sem=recv_sems.at[6 * n + 3 * b + k], device_id=to, device_id_type=MESH)

        me, sibling = (x, y, c), (x, y, 1 - c)
        first = [copy(a, j, 2 * x + y, c, (*chips[j], c), True) for a in range(n) for j in range(3)]
        first += [whole_copy(b, j, 2 * x + y, (*chips[j], c)) for b in range(nw) for j in range(3)]
        landing = [copy(a, j, chip_ids[j], c, me, False) for a in range(n) for j in range(3)]
        passed = [copy(a, 3 + j, chip_ids[j], c, sibling, False) for a in range(n) for j in range(3)]
        arriving = [copy(a, 3 + j, chip_ids[j], 1 - c, me, False) for a in range(n) for j in range(3)]
        arriving += [whole_copy(b, j, chip_ids[j], me) for b in range(nw) for j in range(3)]
        return first, landing, passed, arriving

    def start(*refs):
        for cp in plan(*refs)[0]:
            cp.start()

    def finish(*refs):
        first, landing, passed, arriving = plan(*refs)
        for lands, onward in zip(landing, passed):
            lands.wait_recv()
            onward.start()
        for cp in arriving:
            cp.wait_recv()
        for cp in first + passed:
            cp.wait_send()

    out_shape = [jax.ShapeDtypeStruct((NCHIP,) + s.shape, s.dtype) for s in list(shards) + list(whole)]
    return _Exchange(list(shards) + list(whole), out_shape, 6 * n + 3 * nw, start, finish)


def _with_own(gathered, own):
    x, y, _ = _place()
    return lax.dynamic_update_index_in_dim(gathered, own, 2 * x + y, axis=0)


def _simple_exchange(inputs, out_shape, copies_of):
    def start(*refs):
        for cp in copies_of(*refs):
            cp.start()

    def finish(*refs):
        for cp in copies_of(*refs):
            cp.wait()

    return _Exchange(list(inputs), out_shape, len(out_shape) * 3, start, finish)


def _pair_exchange(grads):
    def copies_of(src, outs, send_sems, recv_sems):
        x, y, c = _place()
        return [pltpu.make_async_remote_copy(
            src_ref=src[a].at[:, _half_rows(src[a].shape[1], 1 - c)], dst_ref=outs[a], send_sem=send_sems.at[a],
            recv_sem=recv_sems.at[a], device_id=(x, y, 1 - c), device_id_type=MESH) for a in range(len(src))]

    return _simple_exchange(
        grads, [jax.ShapeDtypeStruct((g.shape[0], g.shape[1] // 2, g.shape[2]), g.dtype) for g in grads], copies_of)


def _pair_sum(grads, theirs, name):
    n = len(grads)

    def body(*refs):
        south = lax.axis_index("c") == 0
        for a in range(n):
            g = refs[a][...]
            half = g.shape[0] // 2
            mine = jnp.where(south, g[:half], g[half:])
            refs[2 * n + a][...] = (mine.astype(F32) + refs[n + a][...].astype(F32)).astype(BF16)

    def specs(arrs):
        return [pl.BlockSpec((None,) + g.shape[1:], lambda j: (j, 0, 0)) for g in arrs]

    return pl.pallas_call(
        body, name=name, grid=(NCHIP,), in_specs=specs(grads) + specs(theirs), out_specs=specs(theirs),
        out_shape=[jax.ShapeDtypeStruct(g.shape, BF16) for g in theirs], compiler_params=_cparams(),
    )(*grads, *theirs)


def _chip_exchange(parts):
    def copies_of(src, outs, send_sems, recv_sems):
        x, y, c = _place()
        return [pltpu.make_async_remote_copy(
            src_ref=src[a].at[2 * chip[0] + chip[1]], dst_ref=outs[a].at[k], send_sem=send_sems.at[3 * a + k],
            recv_sem=recv_sems.at[3 * a + k], device_id=(*chip, c), device_id_type=MESH)
            for a in range(len(src)) for k, chip in enumerate(_other_chips(x, y))]

    return _simple_exchange(parts, [jax.ShapeDtypeStruct((NCHIP - 1,) + p.shape[1:], p.dtype) for p in parts], copies_of)


def _chip_sum(parts, received, exchange=None):
    n = len(parts)
    steps = 4
    ex_in, ex_in_specs, ex_out_specs, ex_out_shape, ex_scratch = _hosted(exchange)

    def body(*refs):
        ex_refs = refs[2 * n:2 * n + len(ex_in)] + refs[3 * n + len(ex_in):]
        if exchange is not None:
            @pl.when(pl.program_id(0) == 0)
            def _():
                exchange.start(*exchange.split(ex_refs))

        chip = 2 * lax.axis_index("x") + lax.axis_index("y")
        for a in range(n):
            p, r = refs[a], refs[n + a]
            own = jnp.where(chip == 0, p[0], jnp.where(chip == 1, p[1], jnp.where(chip == 2, p[2], p[3])))
            refs[2 * n + len(ex_in) + a][...] = ((own.astype(F32) + r[0].astype(F32)) + r[1].astype(F32)) + r[2].astype(F32)

        if exchange is not None:
            @pl.when(pl.program_id(0) == steps - 1)
            def _():
                exchange.finish(*exchange.split(ex_refs))

    def specs(arrs):
        return [pl.BlockSpec((g.shape[0], g.shape[1] // steps, g.shape[2]), lambda i: (0, i, 0)) for g in arrs]

    out_specs = [pl.BlockSpec((g.shape[1] // steps, g.shape[2]), lambda i: (i, 0)) for g in parts]
    res = pl.pallas_call(
        body, name="grads_chip_sum", grid=(steps,), in_specs=specs(parts) + specs(received) + ex_in_specs,
        out_specs=out_specs + ex_out_specs, out_shape=[jax.ShapeDtypeStruct(g.shape[1:], F32) for g in parts] + ex_out_shape,
        scratch_shapes=ex_scratch, compiler_params=_cparams(),
    )(*parts, *received, *ex_in)
    return res[:n], res[n:]


def _pair_share(halves):
    def copies_of(src, outs, send_sems, recv_sems):
        x, y, c = _place()
        return [pltpu.make_async_remote_copy(
            src_ref=src[a], dst_ref=outs[a], send_sem=send_sems.at[a], recv_sem=recv_sems.at[a],
            device_id=(x, y, 1 - c), device_id_type=MESH) for a in range(len(src))]

    return _simple_exchange(halves, [jax.ShapeDtypeStruct(h.shape, F32) for h in halves], copies_of)


def _adamw_math(w, g, m, v):
    nm = ADAM_B1 * m + (1.0 - ADAM_B1) * g
    nv = ADAM_B2 * v + (1.0 - ADAM_B2) * jnp.square(g)
    m_hat = nm / (1.0 - ADAM_B1 ** ADAM_STEP)
    v_hat = nv / (1.0 - ADAM_B2 ** ADAM_STEP)
    return -ADAM_LR * (m_hat / (jnp.sqrt(v_hat) + ADAM_EPS) + ADAM_WD * w), nm, nv


def _adamw_big(ws, g_mine, g_theirs, ms, vs, exchange=None):
    n = len(ws)
    steps = 8
    ex_in, ex_in_specs, ex_out_specs, ex_out_shape, ex_scratch = _hosted(exchange)

    def body(*refs):
        ex_refs = refs[5 * n:5 * n + len(ex_in)] + refs[9 * n + len(ex_in):]
        outs = refs[5 * n + len(ex_in):9 * n + len(ex_in)]
        if exchange is not None:
            @pl.when(pl.program_id(0) == 0)
            def _():
                exchange.start(*exchange.split(ex_refs))

        own_half = (pl.program_id(0) // (steps // 2)) == lax.axis_index("c")
        for a in range(n):
            g = jnp.where(own_half, refs[n + a][...], refs[2 * n + a][...])
            d, nm, nv = _adamw_math(refs[a][...], g, refs[3 * n + a][...], refs[4 * n + a][...])
            outs[a][...] = g
            outs[n + a][...] = d
            outs[2 * n + a][...] = nm
            outs[3 * n + a][...] = nv

        if exchange is not None:
            @pl.when(pl.program_id(0) == steps - 1)
            def _():
                exchange.finish(*exchange.split(ex_refs))

    specs = [pl.BlockSpec((w.shape[0] // steps, w.shape[1]), lambda i: (i, 0)) for w in ws]
    half_specs = [pl.BlockSpec((g.shape[0] // (steps // 2), g.shape[1]), lambda i: (i % (steps // 2), 0)) for g in g_mine]
    shapes = [jax.ShapeDtypeStruct(w.shape, F32) for w in ws]
    res = pl.pallas_call(
        body, name="adamw_big", grid=(steps,), in_specs=specs + half_specs * 2 + specs * 2 + ex_in_specs,
        out_specs=specs * 4 + ex_out_specs, out_shape=shapes * 4 + ex_out_shape, scratch_shapes=ex_scratch,
        compiler_params=_cparams(),
    )(*ws, *g_mine, *g_theirs, *ms, *vs, *ex_in)
    return res[:n], res[n:2 * n], res[2 * n:3 * n], res[3 * n:4 * n], res[4 * n:]


def _adamw_in(w, g_mine, g_theirs, m, v):
    half = D // 2

    def body(w_ref, gm_ref, gt_ref, m_ref, v_ref, g_out, d_out, nm_out, nv_out, g_ref):
        south = lax.axis_index("c") == 0
        g_ref[0:half, :] = jnp.where(south, gm_ref[...], gt_ref[...])
        g_ref[half:D, :] = jnp.where(south, gt_ref[...], gm_ref[...])
        g = g_ref[0:CW, :]
        d, nm, nv = _adamw_math(w_ref[...], g, m_ref[...], v_ref[...])
        g_out[...] = g
        d_out[...] = d
        nm_out[...] = nm
        nv_out[...] = nv

    spec = pl.BlockSpec((CW, LANES), lambda i: (0, i))
    half_spec = pl.BlockSpec((half, LANES), lambda i: (0, i))
    return pl.pallas_call(
        body, name="adamw_in", grid=(D // LANES,), in_specs=[spec, half_spec, half_spec, spec, spec], out_specs=[spec] * 4,
        out_shape=[jax.ShapeDtypeStruct((CW, D), F32)] * 4, scratch_shapes=[pltpu.VMEM((D, LANES), F32)],
        compiler_params=_cparams(),
    )(w, g_mine, g_theirs, m, v)


NORM_NAMES = ("pre_mix_norm", "post_mix_norm", "pre_mlp_norm", "post_mlp_norm")
SMALL_NAMES = NORM_NAMES + ("gdn_conv_w", "fox_f_bias", "gdn_dt_bias", "gdn_a_log", "fox_out_norm", "gdn_out_norm")
CONV_COLS = 3 * DGDN // NCHIP


def _small_gather(d_norms, d_conv, sums, d_fox_norm, d_gdn_norm):
    n_remote = 5 * (NDEV - 1)

    def copies_of(src, outs, send_sems, recv_sems):
        x, y, c = _place()
        me = 4 * x + 2 * y + c

        def from_me(chip_index):
            cols = pl.ds(pl.multiple_of(chip_index * CONV_COLS, LANES), CONV_COLS)
            return [src[0], src[1].at[:, cols], src[2], src[3], src[4]]

        local = [pltpu.make_async_copy(s, outs[a].at[me], send_sems.at[n_remote + a]) for a, s in enumerate(from_me(2 * x + y))]
        remote = []
        for k in range(1, NDEV):
            px, py, pc = x ^ ((k >> 2) & 1), y ^ ((k >> 1) & 1), c ^ (k & 1)
            remote += [pltpu.make_async_remote_copy(
                src_ref=s, dst_ref=outs[a].at[me], send_sem=send_sems.at[5 * (k - 1) + a],
                recv_sem=recv_sems.at[5 * (k - 1) + a], device_id=(px, py, pc), device_id_type=MESH)
                for a, s in enumerate(from_me(2 * px + py))]
        return local + remote

    def start(*refs):
        for cp in copies_of(*refs):
            cp.start()

    def finish(*refs):
        for cp in copies_of(*refs):
            cp.wait()

    shapes = [(4, D), (CONV_K, CONV_COLS), (8, LANES), (1, LANES), (1, LANES)]
    return _Exchange([d_norms, d_conv, sums, d_fox_norm, d_gdn_norm],
                     [jax.ShapeDtypeStruct((NDEV,) + s, F32) for s in shapes], n_remote + 5, start, finish)


def _small_adamw(gathered, ws, ms, vs):
    n = len(SMALL_NAMES)

    def body(*refs):
        def total(buf):
            acc = buf[0]
            for i in range(1, NDEV):
                acc = acc + buf[i]
            return acc

        t_norms, t_conv, t_sums, t_fn, t_gn = [total(r) for r in refs[:5]]
        w_refs, m_refs, v_refs = refs[5:5 + n], refs[5 + n:5 + 2 * n], refs[5 + 2 * n:5 + 3 * n]
        outs = refs[5 + 3 * n:]
        grads = [t_norms[i:i + 1, :] for i in range(4)] + [
            t_conv, t_sums[0:1, 0:NFH], t_sums[1:2, 0:NGH], t_sums[2:3, 0:NGH], t_fn[:, 0:FHD], t_gn]
        for a in range(n):
            d, nm, nv = _adamw_math(w_refs[a][...], grads[a], m_refs[a][...], v_refs[a][...])
            outs[a][...] = grads[a]
            outs[n + a][...] = d
            outs[2 * n + a][...] = nm
            outs[3 * n + a][...] = nv

    def whole(arr):
        return pl.BlockSpec(arr.shape, lambda i: (0,) * arr.ndim)

    res = pl.pallas_call(
        body, name="small_adamw", grid=(1,), in_specs=[whole(t) for t in gathered] + [whole(w) for w in ws] * 3,
        out_specs=[whole(w) for w in ws] * 4, out_shape=[jax.ShapeDtypeStruct(w.shape, F32) for w in ws] * 4,
        compiler_params=_cparams(),
    )(*gathered, *ws, *ms, *vs)
    return res[:n], res[n:2 * n], res[2 * n:3 * n], res[3 * n:]


CW = DPROJ // NCHIP
PROJ_RUNS = tuple((part * DFOX + hp * LANES, part * DFOX + (hp + 1) * LANES, (3 * hp + part) * LANES)
                  for hp in range(NPAIR) for part in range(3)) + (
    (1536, 1544, BLK_SMALL * LANES), (1544, 3080, BLK_GDN * LANES), (3080, 3088, BLK_SMALL * LANES + 8),
    (3088, 3600, BLK_GZ * LANES))


def _proj_pieces():
    pieces = []
    for lo, hi, at in PROJ_RUNS:
        while lo < hi:
            j = lo // CW
            end = min(hi, (j + 1) * CW)
            pieces.append((j, lo - j * CW, at, end - lo))
            at, lo = at + end - lo, end
    return pieces


RT = 256


def _to_padded_rows(gathered):
    def body(src_ref, out_ref, blocks_ref, rows_ref):
        blocks_ref[...] = src_ref[...].astype(F32)
        rows_ref[...] = jnp.zeros_like(rows_ref)
        for j, start, at, n in _proj_pieces():
            rows_ref[at:at + n, :] = blocks_ref[j, start:start + n, :]
        out_ref[...] = rows_ref[...].astype(out_ref.dtype)

    return pl.pallas_call(
        body, name="proj_rows_in", grid=(D // RT,), in_specs=[pl.BlockSpec((NCHIP, D, RT), lambda i: (0, 0, i))],
        out_specs=pl.BlockSpec((DPROJ_PAD, RT), lambda i: (0, i)), out_shape=jax.ShapeDtypeStruct((DPROJ_PAD, D), gathered.dtype),
        scratch_shapes=[pltpu.VMEM((NCHIP, D, RT), F32), pltpu.VMEM((DPROJ_PAD, RT), F32)], compiler_params=_cparams(),
    )(gathered)


def _from_padded_rows(w):
    def body(src_ref, out_ref, rows_ref, blocks_ref):
        rows_ref[...] = src_ref[...].astype(F32)
        blocks_ref[...] = jnp.zeros_like(blocks_ref)
        for j, start, at, n in _proj_pieces():
            blocks_ref[j, start:start + n, :] = rows_ref[at:at + n, :]
        out_ref[...] = blocks_ref[...].astype(out_ref.dtype)

    return pl.pallas_call(
        body, name="proj_rows_out", grid=(D // RT,), in_specs=[pl.BlockSpec((DPROJ_PAD, RT), lambda i: (0, i))],
        out_specs=pl.BlockSpec((NCHIP, D, RT), lambda i: (0, 0, i)), out_shape=jax.ShapeDtypeStruct((NCHIP, D, D), w.dtype),
        scratch_shapes=[pltpu.VMEM((DPROJ_PAD, RT), F32), pltpu.VMEM((NCHIP, D, RT), F32)], compiler_params=_cparams(),
    )(w)


def _local_step(x, target, first_weights, late_weights, reduce_late, reduce_in, pre_mix_norm, fox_f_bias, fox_out_norm,
                gdn_a_log, gdn_dt_bias, gdn_out_norm, post_mix_norm, pre_mlp_norm, post_mlp_norm):
    bias_vec = jnp.zeros((1, LANES), F32).at[0, 0:NFH].set(fox_f_bias).at[0, LANE_G:LANE_G + NGH].set(gdn_dt_bias)
    alog_vec = jnp.zeros((1, LANES), F32).at[0, LANE_G:LANE_G + NGH].set(gdn_a_log)
    w2 = jnp.concatenate([fox_out_norm, fox_out_norm], axis=1)

    h, first = _pre_norm(x, pre_mix_norm, exchange=first_weights[0])
    win_p, conv_w = first_weights[1](first)
    proj = _matmul(h, win_p, tb=True, tm=2048, tn=768, tk=1024, name="mm_proj")
    gates = _gates(proj, bias_vec, alog_vec)
    mix, fox_o, lse, late_a = _fox_fwd(proj, gates, w2, exchange=late_weights[0])
    qkv = _gdn_pre(proj, conv_w)
    (u, w, qd, kd, a_intra, gl, t_inv), late_b = _gdn_prep(qkv, gates, exchange=late_weights[1])
    wout, wup3, wdown = late_weights[2](late_a, late_b)
    mix, gdn_raw, states = _gdn_scan(u, w, qd, kd, a_intra, gl, proj, gdn_out_norm, mix)
    mixed = _matmul(mix, wout, tm=2048, tk=1024, name="mm_out")
    x1, h2 = _post_mix(x, mixed, post_mix_norm, pre_mlp_norm)

    def relu2(acc):
        r = jnp.maximum(acc, 0.0)
        return r, r * r

    up_relu, act = _matmul(h2, wup3, b3=True, tm=1024, tn=1024, tk=1024, out_dtypes=(BF16, BF16), epilogue=relu2,
                           name="mm_up")
    y = _matmul(act, wdown, tm=1024, tk=DFF, name="mm_down")
    dx2, dy, d_post_mlp, loss_row = _loss_head(x1, y, post_mlp_norm, target)

    dwdown = _matmul(act, dy, ta=True, tm=1024, tn=1024, tk=2048, out_dtypes=(BF16,), name="mm_dwdown")

    def relu2_bwd(acc, r):
        return (acc * 2.0 * r.astype(F32),)

    dup = _matmul(dy, wdown, tb=True, tm=1024, tn=1024, tk=1024, out_dtypes=(BF16,), extra=(up_relu,), epilogue=relu2_bwd,
                  name="mm_dact")
    dwup3 = _matmul(h2, dup, ta=True, tm=1024, tn=1024, tk=2048, out_dtypes=(BF16,), o3=True, name="mm_dwup")
    dh2 = _matmul(dup, wup3, tb=True, b3=True, tm=1024, tk=DFF, name="mm_dh2")
    dx1, dmixed, d_pre_mlp, d_post_mix = _mid_bwd(dh2, x1, pre_mlp_norm, dx2, mixed, post_mix_norm)
    dwout = _matmul(mix, dmixed, ta=True, tm=1024, tn=1024, tk=2048, out_dtypes=(BF16,), name="mm_dwout")
    dmix = _matmul(dmixed, wout, tb=True, tm=2048, tk=1024, name="mm_dmix")

    dfox, delta, d_fox_norm, from_sibling = _fox_norm_bwd(fox_o, dmix, w2, exchange=reduce_late[0](dwout, dwup3, dwdown))
    dproj, dcum_fox, reduced_late = _fox_bwd(proj, dfox, gates, lse, delta, exchange=reduce_late[1](from_sibling))
    dproj, du, dw, dqd, dkd, da, dgl, d_gdn_norm = _gdn_scan_bwd(dmix, gdn_raw, proj, gdn_out_norm, u, w, qd, kd,
                                                                 a_intra, gl, states, dproj)
    dqkv, dgates_gdn = _gdn_prep_bwd(qkv, gates, t_inv, du, dw, dqd, dkd, da, dgl)
    dproj, d_conv = _gdn_pre_bwd(proj, conv_w, dqkv, dproj)
    dproj, sums = _gates_bwd(proj, bias_vec, alog_vec, dgates_gdn, dcum_fox, dproj)

    dwin_p = _matmul(dproj, h, ta=True, tm=1280, tn=1024, tk=2048, out_dtypes=(BF16,), name="mm_dwin")
    exchange_in = reduce_in(dwin_p)
    dh = _matmul(dproj, win_p, tm=1024, tk=DPROJ_PAD, name="mm_dh", exchange=exchange_in)
    dh, reduced_in = dh if exchange_in is not None else (dh, [])
    grad_x, d_pre_mix = _pre_norm_bwd(dh, x, pre_mix_norm, dx1)

    d_norms = jnp.concatenate([d_pre_mix, d_post_mix, d_pre_mlp, d_post_mlp], axis=0)
    return loss_row[0, 0], grad_x, (d_norms, d_conv, sums, d_fox_norm, d_gdn_norm), reduced_late, reduced_in


def kernel(x, pre_mix_norm, w_in, fox_f_bias, fox_out_norm, gdn_conv_w, gdn_a_log, gdn_dt_bias, gdn_out_norm, w_out, post_mix_norm, pre_mlp_norm, w_up, w_down, post_mlp_norm, loss_target, m_pre_mix_norm, m_w_in, m_fox_f_bias, m_fox_out_norm, m_gdn_conv_w, m_gdn_a_log, m_gdn_dt_bias, m_gdn_out_norm, m_w_out, m_post_mix_norm, m_pre_mlp_norm, m_w_up, m_w_down, m_post_mlp_norm, v_pre_mix_norm, v_w_in, v_fox_f_bias, v_fox_out_norm, v_gdn_conv_w, v_gdn_a_log, v_gdn_dt_bias, v_gdn_out_norm, v_w_out, v_post_mix_norm, v_pre_mlp_norm, v_w_up, v_w_down, v_post_mlp_norm):
    weights = dict(pre_mix_norm=pre_mix_norm, w_in=w_in, fox_f_bias=fox_f_bias, fox_out_norm=fox_out_norm, gdn_conv_w=gdn_conv_w,
                   gdn_a_log=gdn_a_log, gdn_dt_bias=gdn_dt_bias, gdn_out_norm=gdn_out_norm, w_out=w_out, post_mix_norm=post_mix_norm,
                   pre_mlp_norm=pre_mlp_norm, w_up=w_up, w_down=w_down, post_mlp_norm=post_mlp_norm)
    m_in = dict(pre_mix_norm=m_pre_mix_norm, w_in=m_w_in, fox_f_bias=m_fox_f_bias, fox_out_norm=m_fox_out_norm, gdn_conv_w=m_gdn_conv_w,
                gdn_a_log=m_gdn_a_log, gdn_dt_bias=m_gdn_dt_bias, gdn_out_norm=m_gdn_out_norm, w_out=m_w_out, post_mix_norm=m_post_mix_norm,
                pre_mlp_norm=m_pre_mlp_norm, w_up=m_w_up, w_down=m_w_down, post_mlp_norm=m_post_mlp_norm)
    v_in = dict(pre_mix_norm=v_pre_mix_norm, w_in=v_w_in, fox_f_bias=v_fox_f_bias, fox_out_norm=v_fox_out_norm, gdn_conv_w=v_gdn_conv_w,
                gdn_a_log=v_gdn_a_log, gdn_dt_bias=v_gdn_dt_bias, gdn_out_norm=v_gdn_out_norm, w_out=v_w_out, post_mix_norm=v_post_mix_norm,
                pre_mlp_norm=v_pre_mlp_norm, w_up=v_w_up, w_down=v_w_down, post_mlp_norm=v_post_mlp_norm)
    order_w = ("pre_mix_norm", "w_in", "fox_f_bias", "fox_out_norm", "gdn_conv_w", "gdn_a_log", "gdn_dt_bias", "gdn_out_norm", "w_out",
               "post_mix_norm", "pre_mlp_norm", "w_up", "w_down", "post_mlp_norm")
    big = ("w_in", "w_out", "w_up", "w_down")

    def row(v):
        return v if v.ndim == 2 else v.reshape(1, -1)

    win_shard = jnp.pad(w_in.T.astype(BF16), ((0, D - CW), (0, 0)))

    def resolve_first(gathered):
        win_g, conv_g = gathered
        return (_to_padded_rows(_with_own(win_g, win_shard)),
                _with_own(conv_g, gdn_conv_w).transpose(1, 0, 2).reshape(CONV_K, 3 * DGDN))

    late_shards = [weights[n].astype(BF16) for n in big[1:]]

    def resolve_late(gathered_a, gathered_b):
        wup3, wdown_g, wout_g = [_with_own(g, own) for g, own in
                                 zip(list(gathered_a) + list(gathered_b), late_shards[1:] + late_shards[:1])]
        return wout_g.reshape(D, D), wup3, wdown_g.reshape(DFF, D)

    pair_sums, late_blocks = {}, []

    def chip_exchange_of(names, blocks, theirs):
        for n, s in zip(names, _pair_sum(blocks, theirs, "grads_pair_sum_" + names[0])):
            pair_sums[n] = s
        return _chip_exchange([pair_sums[n] for n in names])

    def late_pair_exchange(dwout, dwup3, dwdown):
        late_blocks.extend([dwout.reshape(NCHIP, D // NCHIP, D), dwup3, dwdown.reshape(NCHIP, DFF // NCHIP, D)])
        return _pair_exchange(late_blocks)

    def late_chip_exchange(theirs):
        return chip_exchange_of(big[1:], late_blocks, theirs)

    def reduce_in(dwin_p):
        blocks = [_from_padded_rows(dwin_p)]
        return chip_exchange_of(big[:1], blocks, _run_exchange(_pair_exchange(blocks), "grads_pair_exchange_w_in"))

    loss_local, grad_x, small, received_late, received_in = _local_step(
        x[0], loss_target[0], (_allgather_exchange([win_shard], whole=[gdn_conv_w]), resolve_first),
        (_allgather_exchange(late_shards[1:]), _allgather_exchange(late_shards[:1]), resolve_late),
        (late_pair_exchange, late_chip_exchange), reduce_in, row(pre_mix_norm), fox_f_bias, row(fox_out_norm),
        gdn_a_log, gdn_dt_bias,
        row(gdn_out_norm), row(post_mix_norm), row(pre_mlp_norm), row(post_mlp_norm))
    loss = lax.psum(loss_local, ("x", "y", "c"))

    g_mine, small_gathered = _chip_sum([pair_sums[n] for n in big], list(received_in) + list(received_late),
                                       exchange=_small_gather(*small))
    g_theirs = _run_exchange(_pair_share(g_mine), "grads_pair_share")

    g_big, d_big, nm_big, nv_big, _ = _adamw_big(
        [weights[n] for n in big[1:]], g_mine[1:], g_theirs[1:], [m_in[n] for n in big[1:]], [v_in[n] for n in big[1:]])
    in_t = _adamw_in(w_in.T, g_mine[0], g_theirs[0], m_w_in.T, v_w_in.T)
    g_small, d_small, nm_small, nv_small = _small_adamw(
        small_gathered, [row(weights[n]) for n in SMALL_NAMES], [row(m_in[n]) for n in SMALL_NAMES],
        [row(v_in[n]) for n in SMALL_NAMES])

    grads, delta, new_m, new_v = {}, {}, {}, {}
    grads["w_in"], delta["w_in"], new_m["w_in"], new_v["w_in"] = [t.T for t in in_t]
    for i, n in enumerate(big[1:]):
        grads[n], delta[n], new_m[n], new_v[n] = g_big[i], d_big[i], nm_big[i], nv_big[i]
    for i, n in enumerate(SMALL_NAMES):
        shape = weights[n].shape
        grads[n], delta[n], new_m[n], new_v[n] = (g_small[i].reshape(shape), d_small[i].reshape(shape),
                                                  nm_small[i].reshape(shape), nv_small[i].reshape(shape))
    return (loss, grad_x[None], *[grads[n] for n in order_w], *[delta[n] for n in order_w], *[new_m[n] for n in order_w],
            *[new_v[n] for n in order_w])
```

```python
import jax
import jax.numpy as jnp
from jax import lax
from jax.experimental import pallas as pl
from jax.experimental.pallas import tpu as pltpu

F32 = jnp.float32
BF16 = jnp.bfloat16
MESH = pl.DeviceIdType.MESH

S = 2048
D = 1024
NFH, FHD = 8, 64
NPAIR = NFH // 2
NGH, GHD = 4, 128
DFOX = NFH * FHD
DGDN = NGH * GHD
CHUNK = 64
NCH = S // CHUNK
CONV_K = 4
DFF = 4 * D
EPS = 1e-6
DPROJ = 3600
LANES = 128
DPROJ_PAD = 3840
BLK_GDN = 12
BLK_GZ = 24
BLK_SMALL = 28
NCHIP = 4
NDEV = 8
VMEM_LIMIT = 56 * 1024 * 1024

ADAM_LR = 0.001
ADAM_B1 = 0.9
ADAM_B2 = 0.999
ADAM_EPS = 1e-08
ADAM_WD = 0.01
ADAM_STEP = 10


def _cparams(**kw):
    return pltpu.CompilerParams(vmem_limit_bytes=VMEM_LIMIT, **kw)


def _dn(ca, cb):
    return (((ca,), (cb,)), ((), ()))


def _dot(a, b, ca=1, cb=0):
    return lax.dot_general(a.astype(BF16), b.astype(BF16), _dn(ca, cb), preferred_element_type=F32)


def _hdot(a, b, ca=1, cb=0):
    return lax.dot_general(a.astype(F32), b.astype(F32), _dn(ca, cb), precision=lax.Precision.HIGHEST,
                           preferred_element_type=F32)


def _dot3(a, b, ca=1, cb=0):
    a_hi, b_hi = a.astype(BF16), b.astype(BF16)
    a_lo, b_lo = (a - a_hi.astype(F32)).astype(BF16), (b - b_hi.astype(F32)).astype(BF16)
    dn = _dn(ca, cb)
    return (lax.dot_general(a_hi, b_hi, dn, preferred_element_type=F32)
            + (lax.dot_general(a_hi, b_lo, dn, preferred_element_type=F32)
               + lax.dot_general(a_lo, b_hi, dn, preferred_element_type=F32)))


@jax.custom_vjp
def _mm_nn(a, b):
    return _dot(a, b, 1, 0)


def _mm_nn_fwd(a, b):
    return _dot(a, b, 1, 0), (a, b)


def _mm_nn_bwd(res, g):
    a, b = res
    return _dot(g, b, 1, 1), _dot(a, g, 0, 0)


_mm_nn.defvjp(_mm_nn_fwd, _mm_nn_bwd)


@jax.custom_vjp
def _mm_nt(a, b):
    return _dot(a, b, 1, 1)


def _mm_nt_fwd(a, b):
    return _dot(a, b, 1, 1), (a, b)


def _mm_nt_bwd(res, g):
    a, b = res
    return _dot(g, b, 1, 0), _dot(g, a, 0, 0)


_mm_nt.defvjp(_mm_nt_fwd, _mm_nt_bwd)


@jax.custom_vjp
def _saved_inverse(m, t_inv):
    del m
    return t_inv


def _saved_inverse_fwd(m, t_inv):
    del m
    return t_inv, t_inv


def _saved_inverse_bwd(t_inv, g):
    return -_dot3(_dot3(t_inv, g, 0, 0), t_inv, 1, 1), jnp.zeros_like(t_inv)


_saved_inverse.defvjp(_saved_inverse_fwd, _saved_inverse_bwd)


def _sigmoid(z):
    return 1.0 / (1.0 + jnp.exp(-z))


def _softplus(z):
    return jnp.maximum(z, 0.0) + jnp.log(1.0 + jnp.exp(-jnp.abs(z)))


def _silu(z):
    return z * _sigmoid(z)


def _rms_scale(x):
    return lax.rsqrt(jnp.mean(x * x, axis=-1, keepdims=True) + EPS)


def _rms_bwd(x, w, g):
    r = _rms_scale(x)
    gw = g * w
    dx = r * gw - x * (r * r * r) * jnp.mean(gw * x, axis=-1, keepdims=True)
    return dx, g * x * r


def _matmul(a, b, *, name, ta=False, tb=False, tm=512, tn=512, tk=512, out_dtypes=(F32,), b3=False, o3=False,
            extra=(), epilogue=None, exchange=None):
    m, k = (a.shape[1], a.shape[0]) if ta else a.shape
    if b3:
        n = b.shape[1] if tb else b.shape[0] * b.shape[2]
        kb = b.shape[0] * b.shape[2] if tb else b.shape[1]
    else:
        n, kb = (b.shape[0], b.shape[1]) if tb else (b.shape[1], b.shape[0])
    assert kb == k, (name, kb, k)
    tm, tn, tk = min(tm, m), min(tn, n), min(tk, k)
    assert m % tm == 0 and n % tn == 0 and k % tk == 0, (name, m, n, k, tm, tn, tk)
    nk = k // tk
    whole_k_blocks = b3 and tb and not ta and nk == 1 and b.shape[0] > 1
    n_extra = len(extra)
    n_out = len(out_dtypes)
    grid = (m // tm, n // tn, nk)
    ex_in, ex_in_specs, ex_out_specs, ex_out_shape, ex_scratch = _hosted(exchange)

    def body(*refs):
        a_ref, b_ref = refs[0], refs[1]
        extra_refs = refs[2:2 + n_extra]
        first_out = 2 + n_extra + len(ex_in)
        out_refs = refs[first_out:first_out + n_out]
        ex_refs = refs[2 + n_extra:first_out] + refs[first_out + n_out:first_out + n_out + len(ex_out_shape)] + refs[-2:]
        step = [pl.program_id(d) for d in range(3)]

        if exchange is not None:
            @pl.when((step[0] == 0) & (step[1] == 0) & (step[2] == 0))
            def _():
                exchange.start(*exchange.split(ex_refs))

        def finish(acc):
            outs = (acc,) if epilogue is None else epilogue(acc, *[r[...] for r in extra_refs])
            for o_ref, val in zip(out_refs, outs):
                o_ref[...] = val.astype(o_ref.dtype)

        if whole_k_blocks:
            width = b.shape[2]
            part = _dot(a_ref[:, 0:width], b_ref[0], 1, 1)
            for blk in range(1, b.shape[0]):
                part = part + _dot(a_ref[:, blk * width:(blk + 1) * width], b_ref[blk], 1, 1)
        else:
            part = _dot(a_ref[...], b_ref[...], 0 if ta else 1, 1 if tb else 0)
        if nk == 1:
            finish(part)
        else:
            acc_ref = refs[first_out + n_out + len(ex_out_shape)]

            @pl.when(step[2] == 0)
            def _():
                acc_ref[...] = part

            @pl.when(step[2] > 0)
            def _():
                acc_ref[...] += part

            @pl.when(step[2] == nk - 1)
            def _():
                finish(acc_ref[...])

        if exchange is not None:
            @pl.when((step[0] == grid[0] - 1) & (step[1] == grid[1] - 1) & (step[2] == nk - 1))
            def _():
                exchange.finish(*exchange.split(ex_refs))

    a_spec = pl.BlockSpec((tk, tm), lambda i, j, kk: (kk, i)) if ta else pl.BlockSpec((tm, tk), lambda i, j, kk: (i, kk))
    if whole_k_blocks:
        b_spec = pl.BlockSpec((b.shape[0], tn, b.shape[2]), lambda i, j, kk: (0, j, 0))
    elif b3 and tb:
        assert b.shape[2] == tk
        b_spec = pl.BlockSpec((None, tn, tk), lambda i, j, kk: (kk, j, 0))
    elif b3:
        assert b.shape[2] == tn
        b_spec = pl.BlockSpec((None, tk, tn), lambda i, j, kk: (j, kk, 0))
    elif tb:
        b_spec = pl.BlockSpec((tn, tk), lambda i, j, kk: (j, kk))
    else:
        b_spec = pl.BlockSpec((tk, tn), lambda i, j, kk: (kk, j))
    tile = pl.BlockSpec((tm, tn), lambda i, j, kk: (i, j))
    out_specs = [tile] * n_out
    out_shape = [jax.ShapeDtypeStruct((m, n), dt) for dt in out_dtypes]
    if o3:
        out_specs[0] = pl.BlockSpec((None, tm, tn), lambda i, j, kk: (j, i, 0))
        out_shape[0] = jax.ShapeDtypeStruct((n // tn, m, tn), out_dtypes[0])
    res = pl.pallas_call(
        body, name=name, grid=grid,
        in_specs=[a_spec, b_spec] + [tile] * n_extra + ex_in_specs, out_specs=out_specs + ex_out_specs,
        out_shape=out_shape + ex_out_shape,
        scratch_shapes=([pltpu.VMEM((tm, tn), F32)] if nk > 1 else []) + ex_scratch,
        compiler_params=_cparams(),
    )(a, b, *extra, *ex_in)
    if exchange is not None:
        return (res[0] if n_out == 1 else res[:n_out]), res[n_out:]
    return res[0] if n_out == 1 else res


TR = 256


def _row_spec(cols):
    return pl.BlockSpec((TR, cols), lambda i: (i, 0))


def _vec_spec(cols):
    return pl.BlockSpec((1, cols), lambda i: (0, 0))


def _pre_norm(x, w, exchange=None):
    ex_in, ex_in_specs, ex_out_specs, ex_out_shape, ex_scratch = _hosted(exchange)

    def body(*refs):
        x_ref, w_ref, h_ref = refs[0], refs[1], refs[2 + len(ex_in)]
        ex_refs = refs[2:2 + len(ex_in)] + refs[3 + len(ex_in):]
        if exchange is not None:
            @pl.when(pl.program_id(0) == 0)
            def _():
                exchange.start(*exchange.split(ex_refs))

        xv = x_ref[...]
        h_ref[...] = (xv * _rms_scale(xv) * w_ref[...]).astype(BF16)

        if exchange is not None:
            @pl.when(pl.program_id(0) == S // TR - 1)
            def _():
                exchange.finish(*exchange.split(ex_refs))

    res = pl.pallas_call(
        body, name="pre_norm", grid=(S // TR,), in_specs=[_row_spec(D), _vec_spec(D)] + ex_in_specs,
        out_specs=[_row_spec(D)] + ex_out_specs, out_shape=[jax.ShapeDtypeStruct((S, D), BF16)] + ex_out_shape,
        scratch_shapes=ex_scratch, compiler_params=_cparams(),
    )(x, w, *ex_in)
    return res[0], res[1:]


def _post_mix(x, mixed, w_post, w_pre_mlp):
    def body(x_ref, m_ref, wp_ref, wm_ref, x1_ref, h2_ref):
        mv = m_ref[...]
        x1 = x_ref[...] + mv * _rms_scale(mv) * wp_ref[...]
        x1_ref[...] = x1
        h2_ref[...] = (x1 * _rms_scale(x1) * wm_ref[...]).astype(BF16)

    return pl.pallas_call(
        body, name="post_mix", grid=(S // TR,),
        in_specs=[_row_spec(D), _row_spec(D), _vec_spec(D), _vec_spec(D)], out_specs=[_row_spec(D), _row_spec(D)],
        out_shape=[jax.ShapeDtypeStruct((S, D), F32), jax.ShapeDtypeStruct((S, D), BF16)], compiler_params=_cparams(),
    )(x, mixed, w_post, w_pre_mlp)


def _loss_head(x1, y, w_post_mlp, target):
    def body(x1_ref, y_ref, w_ref, t_ref, dx2_ref, dy_ref, dw_ref, loss_ref):
        i = pl.program_id(0)
        yv = y_ref[...]
        w = w_ref[...]
        x2 = x1_ref[...] + yv * _rms_scale(yv) * w
        err = x2 - t_ref[...]
        dx2 = err * (1.0 / D)
        dx2_ref[...] = dx2
        dy, dwt = _rms_bwd(yv, w, dx2)
        dy_ref[...] = dy.astype(BF16)

        @pl.when(i == 0)
        def _():
            dw_ref[...] = jnp.zeros_like(dw_ref)
            loss_ref[...] = jnp.zeros_like(loss_ref)

        dw_ref[...] += jnp.sum(dwt, axis=0, keepdims=True)
        part = 0.5 * jnp.sum(jnp.mean(err * err, axis=-1, keepdims=True), axis=0, keepdims=True)
        loss_ref[...] += jnp.broadcast_to(part, loss_ref.shape)

    return pl.pallas_call(
        body, name="loss_head", grid=(S // TR,),
        in_specs=[_row_spec(D), _row_spec(D), _vec_spec(D), _row_spec(D)],
        out_specs=[_row_spec(D), _row_spec(D), _vec_spec(D), _vec_spec(LANES)],
        out_shape=[jax.ShapeDtypeStruct((S, D), F32), jax.ShapeDtypeStruct((S, D), BF16),
                   jax.ShapeDtypeStruct((1, D), F32), jax.ShapeDtypeStruct((1, LANES), F32)],
        compiler_params=_cparams(),
    )(x1, y, w_post_mlp, target)


def _mid_bwd(dh2, x1, w_pre_mlp, dx2, mixed, w_post):
    def body(dh2_ref, x1_ref, wm_ref, dx2_ref, m_ref, wp_ref, dx1_ref, dm_ref, dwm_ref, dwp_ref):
        i = pl.program_id(0)
        dxa, dwm = _rms_bwd(x1_ref[...], wm_ref[...], dh2_ref[...])
        dx1 = dx2_ref[...] + dxa
        dx1_ref[...] = dx1
        dm, dwp = _rms_bwd(m_ref[...], wp_ref[...], dx1)
        dm_ref[...] = dm.astype(BF16)

        @pl.when(i == 0)
        def _():
            dwm_ref[...] = jnp.zeros_like(dwm_ref)
            dwp_ref[...] = jnp.zeros_like(dwp_ref)

        dwm_ref[...] += jnp.sum(dwm, axis=0, keepdims=True)
        dwp_ref[...] += jnp.sum(dwp, axis=0, keepdims=True)

    return pl.pallas_call(
        body, name="mid_bwd", grid=(S // TR,),
        in_specs=[_row_spec(D), _row_spec(D), _vec_spec(D), _row_spec(D), _row_spec(D), _vec_spec(D)],
        out_specs=[_row_spec(D), _row_spec(D), _vec_spec(D), _vec_spec(D)],
        out_shape=[jax.ShapeDtypeStruct((S, D), F32), jax.ShapeDtypeStruct((S, D), BF16),
                   jax.ShapeDtypeStruct((1, D), F32), jax.ShapeDtypeStruct((1, D), F32)],
        compiler_params=_cparams(),
    )(dh2, x1, w_pre_mlp, dx2, mixed, w_post)


def _pre_norm_bwd(dh, x, w, dx1):
    def body(dh_ref, x_ref, w_ref, dx1_ref, dx_ref, dw_ref):
        i = pl.program_id(0)
        dxa, dwt = _rms_bwd(x_ref[...], w_ref[...], dh_ref[...])
        dx_ref[...] = dx1_ref[...] + dxa

        @pl.when(i == 0)
        def _():
            dw_ref[...] = jnp.zeros_like(dw_ref)

        dw_ref[...] += jnp.sum(dwt, axis=0, keepdims=True)

    return pl.pallas_call(
        body, name="pre_norm_bwd", grid=(S // TR,),
        in_specs=[_row_spec(D), _row_spec(D), _vec_spec(D), _row_spec(D)], out_specs=[_row_spec(D), _vec_spec(D)],
        out_shape=[jax.ShapeDtypeStruct((S, D), F32), jax.ShapeDtypeStruct((1, D), F32)], compiler_params=_cparams(),
    )(dh, x, w, dx1)


BQ = 512
NQ = S // BQ
LANE_BETA, LANE_G = 8, 12


def _gate_lanes(shape):
    lane = lax.broadcasted_iota(jnp.int32, shape, 1)
    return lane < LANE_BETA, (lane >= LANE_BETA) & (lane < LANE_G), (lane >= LANE_G) & (lane < LANE_G + NGH)


def _gates(proj, bias_vec, alog_vec):
    def body(s_ref, b_ref, a_ref, o_ref, carry_ref):
        i = pl.program_id(0)

        @pl.when(i == 0)
        def _():
            carry_ref[...] = jnp.zeros_like(carry_ref)

        z = s_ref[...] + b_ref[...]
        tail = jnp.log(1.0 + jnp.exp(-jnp.abs(z)))
        sp = jnp.maximum(z, 0.0) + tail
        lf = jnp.minimum(z, 0.0) - tail
        r = lax.broadcasted_iota(jnp.int32, (BQ, BQ), 0)
        c = lax.broadcasted_iota(jnp.int32, (BQ, BQ), 1)
        tri = (c <= r).astype(F32)
        cum = _hdot(tri, lf) + carry_ref[...]
        carry_ref[...] = cum[BQ - 1:BQ, :]
        is_fox, is_beta, is_g = _gate_lanes(z.shape)
        o_ref[...] = jnp.where(is_fox, cum, jnp.where(is_beta, _sigmoid(z), jnp.where(is_g, -jnp.exp(a_ref[...]) * sp, 0.0)))

    return pl.pallas_call(
        body, name="gates", grid=(NQ,),
        in_specs=[pl.BlockSpec((BQ, LANES), lambda i: (i, BLK_SMALL)), _vec_spec(LANES), _vec_spec(LANES)],
        out_specs=pl.BlockSpec((BQ, LANES), lambda i: (i, 0)), out_shape=jax.ShapeDtypeStruct((S, LANES), F32),
        scratch_shapes=[pltpu.VMEM((1, LANES), F32)], compiler_params=_cparams(),
    )(proj, bias_vec, alog_vec)


def _gates_bwd(proj, bias_vec, alog_vec, dgates_gdn, dcum_fox, dproj):
    def body(s_ref, b_ref, a_ref, dg_ref, dc_ref, dproj_in, dproj_ref, red_ref, carry_ref):
        del dproj_in
        i = pl.program_id(0)

        @pl.when(i == 0)
        def _():
            carry_ref[...] = jnp.zeros_like(carry_ref)
            red_ref[...] = jnp.zeros_like(red_ref)

        z = s_ref[...] + b_ref[...]
        dg = dg_ref[...] + dc_ref[...]
        r = lax.broadcasted_iota(jnp.int32, (BQ, BQ), 0)
        c = lax.broadcasted_iota(jnp.int32, (BQ, BQ), 1)
        upper = (c >= r).astype(F32)
        dlf = _hdot(upper, dg) + carry_ref[...]
        carry_ref[...] = dlf[0:1, :]
        sig = _sigmoid(z)
        g_scale = -jnp.exp(a_ref[...])
        is_fox, is_beta, is_g = _gate_lanes(z.shape)
        ds = jnp.where(is_fox, dlf * (1.0 - sig), jnp.where(is_beta, dg * sig * (1.0 - sig), jnp.where(is_g, dg * g_scale * sig, 0.0)))
        dproj_ref[:, 0:LANES] = ds.astype(BF16)
        dproj_ref[:, LANES:2 * LANES] = jnp.zeros((BQ, LANES), BF16)
        dalog = jnp.where(is_g, dg * g_scale * _softplus(z), 0.0)
        sums = jnp.sum(ds, axis=0, keepdims=True)
        red_ref[0:1, :] += jnp.where(is_fox[0:1], sums, 0.0)
        red_ref[1:2, :] += pltpu.roll(jnp.where(is_g[0:1], sums, 0.0), LANES - LANE_G, 1)
        red_ref[2:3, :] += pltpu.roll(jnp.sum(dalog, axis=0, keepdims=True), LANES - LANE_G, 1)

    blk = pl.BlockSpec((BQ, LANES), lambda i: (NQ - 1 - i, 0))
    return pl.pallas_call(
        body, name="gates_bwd", grid=(NQ,),
        in_specs=[pl.BlockSpec((BQ, LANES), lambda i: (NQ - 1 - i, BLK_SMALL)), _vec_spec(LANES), _vec_spec(LANES), blk, blk,
                  pl.BlockSpec(memory_space=pl.ANY)],
        out_specs=[pl.BlockSpec((BQ, 2 * LANES), lambda i: (NQ - 1 - i, BLK_SMALL // 2)), pl.BlockSpec((8, LANES), lambda i: (0, 0))],
        out_shape=[jax.ShapeDtypeStruct((S, DPROJ_PAD), BF16), jax.ShapeDtypeStruct((8, LANES), F32)],
        input_output_aliases={5: 0},
        scratch_shapes=[pltpu.VMEM((1, LANES), F32)], compiler_params=_cparams(),
    )(proj, bias_vec, alog_vec, dgates_gdn, dcum_fox, dproj)


FOX_SCALE = FHD ** -0.5
FOX_PAIRS = 2
FOX_PAIRS_BWD = 2


def _head_mask(e):
    lane = lax.broadcasted_iota(jnp.int32, (1, LANES), 1)
    return (lane >= e * FHD) & (lane < (e + 1) * FHD)


def _lane_col(vals, index):
    lane = lax.broadcasted_iota(jnp.int32, vals.shape, 1)
    return jnp.sum(jnp.where(lane == index, vals, 0.0), axis=1, keepdims=True)


def _sublane_row(vals, index):
    row = lax.broadcasted_iota(jnp.int32, vals.shape, 0)
    return jnp.sum(jnp.where(row == index, vals, 0.0), axis=0, keepdims=True)


def _pair_cols(c0, c1):
    lane = lax.broadcasted_iota(jnp.int32, (c0.shape[0], 2), 1)
    return jnp.where(lane == 0, c0, c1)


def _split3(x):
    hi = x.astype(BF16).astype(F32)
    rest = x - hi
    mid = rest.astype(BF16).astype(F32)
    return hi, mid, (rest - mid).astype(BF16).astype(F32)


def _fox_operand(vals, e, cum, is_query):
    lane = lax.broadcasted_iota(jnp.int32, (1, LANES), 1)
    base = (1 - e) * FHD
    parts = _split3(cum)
    own = jnp.where(_head_mask(e), vals * FOX_SCALE if is_query else vals, 0.0)
    cum_at, ones_at = (base, base + 3) if is_query else (base + 3, base)
    sign = 1.0 if is_query else -1.0
    out = own + jnp.where((lane >= ones_at) & (lane < ones_at + 3), 1.0, 0.0)
    for i, part in enumerate(parts):
        out = out + jnp.where(lane == cum_at + i, sign * part, 0.0)
    return out.astype(BF16)


def _causal_block():
    return lax.broadcasted_iota(jnp.int32, (BQ, BQ), 1) <= lax.broadcasted_iota(jnp.int32, (BQ, BQ), 0)


def _head_rms(o, masks):
    o2 = o * o
    r = [lax.rsqrt(jnp.sum(jnp.where(mk, o2, 0.0), axis=1, keepdims=True) * (1.0 / FHD) + EPS) for mk in masks]
    return jnp.where(masks[0], r[0], r[1])


def _hosted(exchange):
    if exchange is None:
        return [], [], [], [], []
    return (exchange.inputs, [HBM] * len(exchange.inputs), [HBM] * len(exchange.out_shape), exchange.out_shape,
            exchange.sem_shapes())


def _fox_fwd(proj, gates, w2, exchange=None):
    ex_in, ex_in_specs, ex_out_specs, ex_out_shape, ex_scratch = _hosted(exchange)

    n_in = 3 * FOX_PAIRS + 2
    heads = [(pp, e) for pp in range(FOX_PAIRS) for e in range(2)]

    def body(*refs):
        qkv_refs, g_ref, w_ref = refs[:3 * FOX_PAIRS], refs[3 * FOX_PAIRS], refs[3 * FOX_PAIRS + 1]
        mix_ref, o_ref, lse_ref = refs[n_in + len(ex_in):n_in + 3 + len(ex_in)]
        ka_ref, vb_ref = refs[n_in + 3 + len(ex_in) + len(ex_out_shape):n_in + 5 + len(ex_in) + len(ex_out_shape)]
        ex_refs = refs[n_in:n_in + len(ex_in)] + refs[n_in + 3 + len(ex_in):n_in + 3 + len(ex_in) + len(ex_out_shape)] + refs[-2:]
        grp, qi = pl.program_id(0), pl.program_id(1)

        def head_index(pp, e):
            return 2 * (FOX_PAIRS * grp + pp) + e

        if exchange is not None:
            @pl.when((grp == 0) & (qi == 0))
            def _():
                exchange.start(*exchange.split(ex_refs))

        @pl.when(qi == 0)
        def _():
            gt = g_ref[...]
            for pp in range(FOX_PAIRS):
                kv = qkv_refs[3 * pp + 1][...]
                for e in range(2):
                    ka_ref[2 * pp + e] = _fox_operand(kv, e, _lane_col(gt, head_index(pp, e)), False)
                vb_ref[pp] = qkv_refs[3 * pp + 2][...].astype(BF16)

        masks = [_head_mask(0), _head_mask(1)]
        gt = g_ref[pl.ds(pl.multiple_of(qi * BQ, BQ), BQ), :]
        qs = [_fox_operand(qkv_refs[3 * pp][...], e, _lane_col(gt, head_index(pp, e)), True) for pp, e in heads]
        n = range(len(heads))

        def block(kj, carry, diagonal):
            rows = pl.ds(pl.multiple_of(kj * BQ, BQ), BQ)
            s = [_dot(qs[i], ka_ref[i, rows, :], 1, 1) for i in n]
            if diagonal:
                s = [jnp.where(_causal_block(), s[i], -jnp.inf) for i in n]
            m_new = [jnp.maximum(carry[i][0], jnp.max(s[i], axis=-1, keepdims=True)) for i in n]
            p = [jnp.exp(s[i] - m_new[i]) for i in n]
            alpha = [jnp.exp(carry[i][0] - m_new[i]) for i in n]
            l_new = [alpha[i] * carry[i][1] + jnp.sum(p[i], axis=-1, keepdims=True) for i in n]
            pv = [_dot(p[i], vb_ref[heads[i][0], rows, :]) for i in n]
            return tuple((m_new[i], l_new[i], alpha[i] * carry[i][2] + pv[i]) for i in n)

        one = (jnp.full((BQ, 1), -jnp.inf, F32), jnp.zeros((BQ, 1), F32), jnp.zeros((BQ, LANES), F32))
        below = lax.fori_loop(0, qi, lambda kj, carry: block(kj, carry, False), (one,) * len(heads))
        done = block(qi, below, True)
        for pp in range(FOX_PAIRS):
            (m0, l0, a0), (m1, l1, a1) = done[2 * pp], done[2 * pp + 1]
            o = jnp.where(masks[0], a0 / l0, a1 / l1)
            cols = slice(pp * LANES, (pp + 1) * LANES)
            o_ref[:, cols] = o
            mix_ref[:, cols] = (o * _head_rms(o, masks) * w_ref[...]).astype(BF16)
            lse_ref[pp] = _pair_cols(m0 + jnp.log(l0), m1 + jnp.log(l1))

        if exchange is not None:
            @pl.when((grp == NPAIR // FOX_PAIRS - 1) & (qi == NQ - 1))
            def _():
                exchange.finish(*exchange.split(ex_refs))

    qkv_specs = []
    for pp in range(FOX_PAIRS):
        qkv_specs.append(pl.BlockSpec((BQ, LANES), lambda g, i, pp=pp: (i, 3 * (FOX_PAIRS * g + pp))))
        qkv_specs.append(pl.BlockSpec((S, LANES), lambda g, i, pp=pp: (0, 3 * (FOX_PAIRS * g + pp) + 1)))
        qkv_specs.append(pl.BlockSpec((S, LANES), lambda g, i, pp=pp: (0, 3 * (FOX_PAIRS * g + pp) + 2)))
    blk = pl.BlockSpec((BQ, FOX_PAIRS * LANES), lambda g, i: (i, g))
    res = pl.pallas_call(
        body, name="fox_fwd", grid=(NPAIR // FOX_PAIRS, NQ),
        in_specs=qkv_specs + [pl.BlockSpec((S, LANES), lambda g, i: (0, 0)), pl.BlockSpec((1, LANES), lambda g, i: (0, 0))]
        + ex_in_specs,
        out_specs=[blk, blk, pl.BlockSpec((FOX_PAIRS, BQ, 2), lambda g, i: (g, i, 0))] + ex_out_specs,
        out_shape=[jax.ShapeDtypeStruct((S, D), BF16), jax.ShapeDtypeStruct((S, DFOX), F32),
                   jax.ShapeDtypeStruct((NPAIR, S, 2), F32)] + ex_out_shape,
        scratch_shapes=[pltpu.VMEM((2 * FOX_PAIRS, S, LANES), BF16), pltpu.VMEM((FOX_PAIRS, S, LANES), BF16)] + ex_scratch,
        compiler_params=_cparams(),
    )(*([proj] * (3 * FOX_PAIRS)), gates, w2, *ex_in)
    return res[0], res[1], res[2], res[3:]


def _fox_norm_bwd(o, dmix, w2, exchange=None):
    ex_in, ex_in_specs, ex_out_specs, ex_out_shape, ex_scratch = _hosted(exchange)

    def body(*refs):
        o_ref, g_ref, w_ref = refs[:3]
        do_ref, dl_ref, dw_ref = refs[3 + len(ex_in):6 + len(ex_in)]
        ex_refs = refs[3:3 + len(ex_in)] + refs[6 + len(ex_in):]
        hp, qi = pl.program_id(0), pl.program_id(1)

        if exchange is not None:
            @pl.when((hp == 0) & (qi == 0))
            def _():
                exchange.start(*exchange.split(ex_refs))

        masks = [_head_mask(0), _head_mask(1)]
        ov = o_ref[...]
        g = g_ref[...]
        r = _head_rms(ov, masks)
        gw = g * w_ref[...]
        gwo = gw * ov
        mean = [jnp.sum(jnp.where(mk, gwo, 0.0), axis=1, keepdims=True) * (1.0 / FHD) for mk in masks]
        do = r * gw - ov * (r * r * r) * jnp.where(masks[0], mean[0], mean[1])
        do_ref[...] = do.astype(BF16)
        doo = do * ov
        dl_ref[...] = _pair_cols(*[jnp.sum(jnp.where(mk, doo, 0.0), axis=1, keepdims=True) for mk in masks])

        @pl.when((hp == 0) & (qi == 0))
        def _():
            dw_ref[...] = jnp.zeros_like(dw_ref)

        dw_ref[...] += jnp.sum(g * ov * r, axis=0, keepdims=True)

        @pl.when((hp == NPAIR - 1) & (qi == NQ - 1))
        def _():
            dw = dw_ref[...]
            dw_ref[...] = dw + pltpu.roll(dw, FHD, 1)
            if exchange is not None:
                exchange.finish(*exchange.split(ex_refs))

    blk = pl.BlockSpec((BQ, LANES), lambda hp, i: (i, hp))
    vec = pl.BlockSpec((1, LANES), lambda hp, i: (0, 0))
    res = pl.pallas_call(
        body, name="fox_norm_bwd", grid=(NPAIR, NQ), in_specs=[blk, blk, vec] + ex_in_specs,
        out_specs=[blk, pl.BlockSpec((None, BQ, 2), lambda hp, i: (hp, i, 0)), vec] + ex_out_specs,
        out_shape=[jax.ShapeDtypeStruct((S, DFOX), BF16), jax.ShapeDtypeStruct((NPAIR, S, 2), F32),
                   jax.ShapeDtypeStruct((1, LANES), F32)] + ex_out_shape,
        scratch_shapes=ex_scratch, compiler_params=_cparams(),
    )(o, dmix, w2, *ex_in)
    return res[0], res[1], res[2], res[3:]


def _fox_bwd(proj, do, gates, lse, delta, exchange=None):
    ex_in, ex_in_specs, ex_out_specs, ex_out_shape, ex_scratch = _hosted(exchange)

    pg = FOX_PAIRS_BWD
    n_in = 3 * pg + 4
    heads = [(pp, e) for pp in range(pg) for e in range(2)]

    def body(*refs):
        qkv_refs = refs[:3 * pg]
        do_ref, g_ref, lse_ref, dl_ref = refs[3 * pg:n_in]
        dproj_ref, dc_ref = refs[n_in + len(ex_in):n_in + 2 + len(ex_in)]
        qa_ref, dq_ref = refs[n_in + 2 + len(ex_in) + len(ex_out_shape):n_in + 4 + len(ex_in) + len(ex_out_shape)]
        ex_refs = refs[n_in:n_in + len(ex_in)] + refs[n_in + 2 + len(ex_in):n_in + 2 + len(ex_in) + len(ex_out_shape)] + refs[-2:]
        grp, kj = pl.program_id(0), pl.program_id(1)

        def head_index(pp, e):
            return 2 * (pg * grp + pp) + e

        if exchange is not None:
            @pl.when((grp == 0) & (kj == 0))
            def _():
                exchange.start(*exchange.split(ex_refs))

        @pl.when(kj == 0)
        def _():
            gt = g_ref[...]
            for pp in range(pg):
                qv = qkv_refs[3 * pp][...]
                for e in range(2):
                    qa_ref[2 * pp + e] = _fox_operand(qv, e, _lane_col(gt, head_index(pp, e)), True)
            dq_ref[...] = jnp.zeros_like(dq_ref)

        @pl.when((grp == 0) & (kj == 0))
        def _():
            dc_ref[...] = jnp.zeros_like(dc_ref)

        masks = [_head_mask(0), _head_mask(1)]
        krows = pl.ds(pl.multiple_of(kj * BQ, BQ), BQ)
        gk = g_ref[krows, :]
        kas = [_fox_operand(qkv_refs[3 * pp + 1][...], e, _lane_col(gk, head_index(pp, e)), False) for pp, e in heads]
        vbs = [qkv_refs[3 * pp + 2][...].astype(BF16) for pp in range(pg)]
        lane = lax.broadcasted_iota(jnp.int32, (BQ, LANES), 1)
        n = range(len(heads))

        def block(qi, carry, diagonal):
            dks, dvs, css = carry
            rows = pl.ds(pl.multiple_of(qi * BQ, BQ), BQ)
            qa = [qa_ref[i, rows, :] for i in n]
            s = [_dot(qa[i], kas[i], 1, 1) for i in n]
            if diagonal:
                s = [jnp.where(_causal_block(), s[i], -jnp.inf) for i in n]
            dov = [do_ref[rows, pp * LANES:(pp + 1) * LANES] for pp in range(pg)]
            doe = [jnp.where(masks[e], dov[pp], jnp.zeros_like(dov[pp])) for pp, e in heads]
            lse2 = [lse_ref[pp, rows, :] for pp in range(pg)]
            dl2 = [dl_ref[pp, rows, :] for pp in range(pg)]
            p = [jnp.exp(s[i] - _lane_col(lse2[heads[i][0]], heads[i][1])) for i in n]
            dp = [_dot(doe[i], vbs[heads[i][0]], 1, 1) for i in n]
            ds = [p[i] * (dp[i] - _lane_col(dl2[heads[i][0]], heads[i][1])) for i in n]
            dv_part = [_dot(p[i], doe[i], 0, 0) for i in n]
            dk_part = [_dot(ds[i], jnp.where(masks[heads[i][1]], qa[i], jnp.zeros_like(qa[i])), 0, 0) for i in n]
            dq_part = [jnp.where(masks[heads[i][1]], _dot(ds[i], kas[i]), 0.0) for i in n]
            css = tuple(css[i] + jnp.sum(ds[i], axis=0, keepdims=True) for i in n)
            dc = jnp.zeros((BQ, LANES), F32)
            for i in n:
                dc = dc + jnp.where(lane == head_index(*heads[i]), jnp.sum(ds[i], axis=1, keepdims=True), 0.0)
            for pp in range(pg):
                dq_ref[pp, rows, :] += (dq_part[2 * pp] + dq_part[2 * pp + 1]) * FOX_SCALE
            dc_ref[rows, :] += dc
            dks = tuple(dks[pp] + dk_part[2 * pp] + dk_part[2 * pp + 1] for pp in range(pg))
            dvs = tuple(dvs[pp] + dv_part[2 * pp] + dv_part[2 * pp + 1] for pp in range(pg))
            return dks, dvs, css

        zero = jnp.zeros((BQ, LANES), F32)
        first = block(kj, ((zero,) * pg, (zero,) * pg, (jnp.zeros((1, BQ), F32),) * len(heads)), True)
        dks, dvs, css = lax.fori_loop(kj + 1, NQ, lambda qi, carry: block(qi, carry, False), first)
        r = lax.broadcasted_iota(jnp.int32, (BQ, BQ), 0)
        c = lax.broadcasted_iota(jnp.int32, (BQ, BQ), 1)
        dcol = jnp.zeros((BQ, LANES), F32)
        for i in n:
            col = jnp.sum(jnp.where(r == c, css[i], 0.0), axis=1, keepdims=True)
            dcol = dcol + jnp.where(lane == head_index(*heads[i]), col, 0.0)
        dc_ref[krows, :] -= dcol
        for pp in range(pg):
            base = 3 * pp * LANES
            dproj_ref[krows, base + LANES:base + 2 * LANES] = dks[pp].astype(BF16)
            dproj_ref[krows, base + 2 * LANES:base + 3 * LANES] = dvs[pp].astype(BF16)

        @pl.when(kj == NQ - 1)
        def _():
            for pp in range(pg):
                dproj_ref[:, 3 * pp * LANES:(3 * pp + 1) * LANES] = dq_ref[pp].astype(BF16)

        if exchange is not None:
            @pl.when((grp == NPAIR // pg - 1) & (kj == NQ - 1))
            def _():
                exchange.finish(*exchange.split(ex_refs))

    qkv_specs = []
    for pp in range(pg):
        qkv_specs.append(pl.BlockSpec((S, LANES), lambda g, j, pp=pp: (0, 3 * (pg * g + pp))))
        qkv_specs.append(pl.BlockSpec((BQ, LANES), lambda g, j, pp=pp: (j, 3 * (pg * g + pp) + 1)))
        qkv_specs.append(pl.BlockSpec((BQ, LANES), lambda g, j, pp=pp: (j, 3 * (pg * g + pp) + 2)))
    pair = pl.BlockSpec((pg, S, 2), lambda g, j: (g, 0, 0))
    res = pl.pallas_call(
        body, name="fox_bwd", grid=(NPAIR // pg, NQ),
        in_specs=qkv_specs + [pl.BlockSpec((S, pg * LANES), lambda g, j: (0, g)), pl.BlockSpec((S, LANES), lambda g, j: (0, 0)),
                              pair, pair] + ex_in_specs,
        out_specs=[pl.BlockSpec((S, 3 * pg * LANES), lambda g, j: (0, g)), pl.BlockSpec((S, LANES), lambda g, j: (0, 0))]
        + ex_out_specs,
        out_shape=[jax.ShapeDtypeStruct((S, DPROJ_PAD), BF16), jax.ShapeDtypeStruct((S, LANES), F32)] + ex_out_shape,
        scratch_shapes=[pltpu.VMEM((2 * pg, S, LANES), BF16), pltpu.VMEM((pg, S, LANES), F32)] + ex_scratch,
        compiler_params=_cparams(),
    )(*([proj] * (3 * pg)), do, gates, lse, delta, *ex_in)
    return res[0], res[1], res[2:]


NQKV = 3 * NGH
GDN_QSCALE = GHD ** -0.5


def _shift_down(x, s):
    if s == 0:
        return x
    row = lax.broadcasted_iota(jnp.int32, x.shape, 0)
    return jnp.where(row >= s, pltpu.roll(x, s, 0), 0.0)


def _shift_up(x, s):
    if s == 0:
        return x
    n = x.shape[0]
    row = lax.broadcasted_iota(jnp.int32, x.shape, 0)
    return jnp.where(row < n - s, pltpu.roll(x, n - s, 0), 0.0)


def _conv_pre(xv, wv):
    pre = xv * wv[CONV_K - 1:CONV_K, :]
    for j in range(CONV_K - 1):
        pre = pre + _shift_down(xv, CONV_K - 1 - j) * wv[j:j + 1, :]
    return pre


def _l2_factors(b):
    return b < 2 * NGH, jnp.where(b < NGH, GDN_QSCALE, 1.0)


def _gdn_pre(proj, conv_w):
    def body(x_ref, w_ref, o_ref):
        b = pl.program_id(0)
        c = _silu(_conv_pre(x_ref[...], w_ref[...]))
        normed, scale = _l2_factors(b)
        rs = lax.rsqrt(jnp.sum(c * c, axis=-1, keepdims=True) + EPS)
        o_ref[...] = c * jnp.where(normed, rs, 1.0) * scale

    return pl.pallas_call(
        body, name="gdn_pre", grid=(NQKV,),
        in_specs=[pl.BlockSpec((S, GHD), lambda b: (0, BLK_GDN + b)), pl.BlockSpec((CONV_K, GHD), lambda b: (0, b))],
        out_specs=pl.BlockSpec((S, GHD), lambda b: (0, b)),
        out_shape=jax.ShapeDtypeStruct((S, NQKV * GHD), F32), compiler_params=_cparams(),
    )(proj, conv_w)


def _gdn_pre_bwd(proj, conv_w, dqkv, dproj):
    def body(x_ref, w_ref, dy_ref, dproj_in, dx_ref, dw_ref):
        del dproj_in
        b = pl.program_id(0)
        xv = x_ref[...]
        wv = w_ref[...]
        pre = _conv_pre(xv, wv)
        sig = _sigmoid(pre)
        c = pre * sig
        normed, scale = _l2_factors(b)
        g = dy_ref[...] * scale
        rs = lax.rsqrt(jnp.sum(c * c, axis=-1, keepdims=True) + EPS)
        dc_n = rs * g - c * (rs * rs * rs) * jnp.sum(g * c, axis=-1, keepdims=True)
        dc = jnp.where(normed, dc_n, g)
        dpre = dc * sig * (1.0 + pre * (1.0 - sig))
        dx = dpre * wv[CONV_K - 1:CONV_K, :]
        for j in range(CONV_K - 1):
            dx = dx + _shift_up(dpre, CONV_K - 1 - j) * wv[j:j + 1, :]
        dx_ref[...] = dx.astype(BF16)
        for j in range(CONV_K):
            dw_ref[j:j + 1, :] = jnp.sum(dpre * _shift_down(xv, CONV_K - 1 - j), axis=0, keepdims=True)

    return pl.pallas_call(
        body, name="gdn_pre_bwd", grid=(NQKV,),
        in_specs=[pl.BlockSpec((S, GHD), lambda b: (0, BLK_GDN + b)), pl.BlockSpec((CONV_K, GHD), lambda b: (0, b)),
                  pl.BlockSpec((None, S, GHD), lambda b: (b // NGH, 0, b % NGH)), pl.BlockSpec(memory_space=pl.ANY)],
        out_specs=[pl.BlockSpec((S, GHD), lambda b: (0, BLK_GDN + b)), pl.BlockSpec((CONV_K, GHD), lambda b: (0, b))],
        out_shape=[jax.ShapeDtypeStruct((S, DPROJ_PAD), BF16), jax.ShapeDtypeStruct((CONV_K, NQKV * GHD), F32)],
        input_output_aliases={3: 0}, compiler_params=_cparams(),
    )(proj, conv_w, dqkv, dproj)


CB = 16
NCB = NCH // CB


def _chunk_prep(qs, ks, vs, gcols, bcols, t_saved=None):
    n = range(len(qs))
    r = lax.broadcasted_iota(jnp.int32, (CHUNK, CHUNK), 0)
    c = lax.broadcasted_iota(jnp.int32, (CHUNK, CHUNK), 1)
    incl = c <= r
    eye = (r == c).astype(F32)
    grow = [jnp.sum(gcols[i] * eye, axis=0, keepdims=True) for i in n]
    gc_col = [jnp.sum(jnp.where(incl, grow[i], 0.0), axis=1, keepdims=True) for i in n]
    gc_row = [jnp.sum(jnp.where(r <= c, gcols[i], 0.0), axis=0, keepdims=True) for i in n]
    decay = [jnp.exp(jnp.where(incl, gc_col[i] - gc_row[i], -jnp.inf)) for i in n]
    kb = [ks[i] * bcols[i] for i in n]
    vb = [vs[i] * bcols[i] for i in n]
    kk = [_mm_nt(kb[i], ks[i]) for i in n]
    m = [jnp.where(c < r, kk[i] * decay[i], 0.0) for i in n]
    if t_saved is None:
        t_inv = [eye - m[i] for i in n]
        p = [_dot3(m[i], m[i]) for i in n]
        for step in range(5):
            t_inv = [t_inv[i] + _dot3(t_inv[i], p[i]) for i in n]
            if step < 4:
                p = [_dot3(p[i], p[i]) for i in n]
    else:
        t_inv = [_saved_inverse(m[i], t_saved[i]) for i in n]
    egc = [jnp.exp(gc_col[i]) for i in n]
    u = [_mm_nn(t_inv[i], vb[i]) for i in n]
    w = [_mm_nn(t_inv[i], kb[i] * egc[i]) for i in n]
    qk = [_mm_nt(qs[i], ks[i]) for i in n]
    gc_last = [gc_col[i][CHUNK - 1:CHUNK, :] for i in n]
    return [(u[i], w[i], qk[i] * decay[i], qs[i] * egc[i], ks[i] * jnp.exp(gc_last[i] - gc_col[i]), jnp.exp(gc_last[i]),
             t_inv[i]) for i in n]


def _prep_specs():
    rows = CB * CHUNK
    qs = pl.BlockSpec((rows, GHD), lambda i, h: (i, h))
    ks = pl.BlockSpec((rows, GHD), lambda i, h: (i, NGH + h))
    vs = pl.BlockSpec((rows, GHD), lambda i, h: (i, 2 * NGH + h))
    gs = pl.BlockSpec((rows, LANES), lambda i, h: (i, 0))
    a_s = pl.BlockSpec((None, rows, CHUNK), lambda i, h: (h, i, 0))
    gl_s = pl.BlockSpec((None, CB, 1, LANES), lambda i, h: (h, i, 0, 0))
    return qs, ks, vs, gs, a_s, gl_s


def _gdn_prep(qkv, gates, exchange=None):
    ex_in, ex_in_specs, ex_out_specs, ex_out_shape, ex_scratch = _hosted(exchange)

    def body(*refs):
        q_ref, k_ref, v_ref, g_ref = refs[:4]
        u_ref, w_ref, qd_ref, kd_ref, a_ref, gl_ref, t_ref = refs[4 + len(ex_in):11 + len(ex_in)]
        ex_refs = refs[4:4 + len(ex_in)] + refs[11 + len(ex_in):]
        h = pl.program_id(1)

        if exchange is not None:
            @pl.when((pl.program_id(0) == 0) & (h == 0))
            def _():
                exchange.start(*exchange.split(ex_refs))

        chunks = [pl.ds(cidx * CHUNK, CHUNK) for cidx in range(CB)]
        gts = [g_ref[rows, :] for rows in chunks]
        outs = _chunk_prep([q_ref[rows, :] for rows in chunks], [k_ref[rows, :] for rows in chunks],
                           [v_ref[rows, :] for rows in chunks], [_lane_col(gt, LANE_G + h) for gt in gts],
                           [_lane_col(gt, LANE_BETA + h) for gt in gts])
        for cidx, rows in enumerate(chunks):
            u, w, a, qd, kd, gl, t_inv = outs[cidx]
            u_ref[rows, :] = u
            w_ref[rows, :] = w
            qd_ref[rows, :] = qd
            kd_ref[rows, :] = kd
            a_ref[rows, :] = a
            t_ref[rows, :] = t_inv
            gl_ref[cidx] = jnp.broadcast_to(gl, (1, LANES))

        if exchange is not None:
            @pl.when((pl.program_id(0) == NCB - 1) & (h == NGH - 1))
            def _():
                exchange.finish(*exchange.split(ex_refs))

    qs, ks, vs, gs, a_s, gl_s = _prep_specs()
    tok = jax.ShapeDtypeStruct((S, DGDN), F32)
    sq = jax.ShapeDtypeStruct((NGH, S, CHUNK), F32)
    res = pl.pallas_call(
        body, name="gdn_prep", grid=(NCB, NGH), in_specs=[qs, ks, vs, gs] + ex_in_specs,
        out_specs=[qs, qs, qs, qs, a_s, gl_s, a_s] + ex_out_specs,
        out_shape=[tok, tok, tok, tok, sq, jax.ShapeDtypeStruct((NGH, NCH, 1, LANES), F32), sq] + ex_out_shape,
        scratch_shapes=ex_scratch, compiler_params=_cparams(),
    )(qkv, qkv, qkv, gates, *ex_in)
    return res[:7], res[7:]


def _gdn_prep_bwd(qkv, gates, t_inv, du, dw, dqd, dkd, da, dgl):
    def body(q_ref, k_ref, v_ref, g_ref, t_ref, du_ref, dw_ref, dqd_ref, dkd_ref, da_ref, dgl_ref, dqkv_ref, dg_ref):
        h = pl.program_id(1)

        @pl.when(h == 0)
        def _():
            dg_ref[...] = jnp.zeros_like(dg_ref)

        lane = lax.broadcasted_iota(jnp.int32, (CHUNK, LANES), 1)
        chunks = [pl.ds(cidx * CHUNK, CHUNK) for cidx in range(CB)]
        gts = [g_ref[rows, :] for rows in chunks]
        t_saved = [t_ref[rows, :] for rows in chunks]
        _, vjp = jax.vjp(lambda *args: [o[:6] for o in _chunk_prep(*args, t_saved=t_saved)],
                         [q_ref[rows, :] for rows in chunks], [k_ref[rows, :] for rows in chunks],
                         [v_ref[rows, :] for rows in chunks], [_lane_col(gt, LANE_G + h) for gt in gts],
                         [_lane_col(gt, LANE_BETA + h) for gt in gts])
        dqs, dks, dvs, dgcs, dbcs = vjp([(du_ref[rows, :], dw_ref[rows, :], da_ref[rows, :], dqd_ref[rows, :],
                                          dkd_ref[rows, :], dgl_ref[cidx][:, 0:1]) for cidx, rows in enumerate(chunks)])
        for cidx, rows in enumerate(chunks):
            dq, dk, dv, dgc, dbc = dqs[cidx], dks[cidx], dvs[cidx], dgcs[cidx], dbcs[cidx]
            dqkv_ref[0, rows, :] = dq
            dqkv_ref[1, rows, :] = dk
            dqkv_ref[2, rows, :] = dv
            dg_ref[rows, :] += jnp.where(lane == LANE_G + h, dgc, 0.0) + jnp.where(lane == LANE_BETA + h, dbc, 0.0)

    qs, ks, vs, gs, a_s, gl_s = _prep_specs()
    return pl.pallas_call(
        body, name="gdn_prep_bwd", grid=(NCB, NGH), in_specs=[qs, ks, vs, gs, a_s, qs, qs, qs, qs, a_s, gl_s],
        out_specs=[pl.BlockSpec((3, CB * CHUNK, GHD), lambda i, h: (0, i, h)), gs],
        out_shape=[jax.ShapeDtypeStruct((3, S, DGDN), F32), jax.ShapeDtypeStruct((S, LANES), F32)],
        compiler_params=_cparams(),
    )(qkv, qkv, qkv, gates, t_inv, du, dw, dqd, dkd, da, dgl)


def _scan_specs(nh, parts, reverse):
    wide, rows, chunks = nh * GHD, S // parts, NCH // parts

    def part(p):
        return parts - 1 - p if reverse else p

    hs = pl.BlockSpec((rows, wide), lambda g, p: (part(p), g))
    a_s = pl.BlockSpec((nh, rows, CHUNK), lambda g, p: (g, part(p), 0))
    gl_s = pl.BlockSpec((nh, chunks, 1, LANES), lambda g, p: (g, part(p), 0, 0))
    st_s = pl.BlockSpec((nh, chunks, GHD, GHD), lambda g, p: (g, part(p), 0, 0))
    gz_s = pl.BlockSpec((rows, wide), lambda g, p: (part(p), BLK_GZ // nh + g))
    mix_s = pl.BlockSpec((rows, wide), lambda g, p: (part(p), NPAIR // nh + g))
    return hs, a_s, gl_s, st_s, gz_s, mix_s


def _head_cols(hh):
    return slice(hh * GHD, (hh + 1) * GHD)


SCAN_HEADS, SCAN_PARTS = 4, 2
SCAN_HEADS_BWD, SCAN_PARTS_BWD = 2, 2


def _gdn_scan(u, w, qd, kd, a, gl, proj, w_norm, mix):
    heads = range(SCAN_HEADS)

    def body(u_ref, w_ref, qd_ref, kd_ref, a_ref, gl_ref, z_ref, wn_ref, mix_in, mix_ref, o_ref, st_ref, carry_ref):
        del mix_in

        @pl.when(pl.program_id(1) == 0)
        def _():
            carry_ref[...] = jnp.zeros_like(carry_ref)

        def step(ci, states):
            rows = pl.ds(pl.multiple_of(ci * CHUNK, CHUNK), CHUNK)
            for hh in heads:
                st_ref[hh, ci] = states[hh]
            ws = [_dot(w_ref[rows, _head_cols(hh)], states[hh]) for hh in heads]
            qs = [_dot(qd_ref[rows, _head_cols(hh)], states[hh]) for hh in heads]
            vn = [u_ref[rows, _head_cols(hh)] - ws[hh] for hh in heads]
            av = [_dot(a_ref[hh, rows, :], vn[hh]) for hh in heads]
            kv = [_dot(kd_ref[rows, _head_cols(hh)], vn[hh], 0, 0) for hh in heads]
            for hh in heads:
                o_ref[rows, _head_cols(hh)] = qs[hh] + av[hh]
            return tuple(states[hh] * gl_ref[hh, ci] + kv[hh] for hh in heads)

        last = lax.fori_loop(0, NCH // SCAN_PARTS, step, tuple(carry_ref[hh] for hh in heads))
        for hh in heads:
            carry_ref[hh] = last[hh]
            ov = o_ref[:, _head_cols(hh)]
            mix_ref[:, _head_cols(hh)] = (ov * _rms_scale(ov) * wn_ref[...] * _silu(z_ref[:, _head_cols(hh)])).astype(BF16)

    hs, a_s, gl_s, st_s, gz_s, mix_s = _scan_specs(SCAN_HEADS, SCAN_PARTS, False)
    return pl.pallas_call(
        body, name="gdn_scan", grid=(NGH // SCAN_HEADS, SCAN_PARTS),
        in_specs=[hs, hs, hs, hs, a_s, gl_s, gz_s, pl.BlockSpec((1, GHD), lambda g, p: (0, 0)),
                  pl.BlockSpec(memory_space=pl.ANY)],
        out_specs=[mix_s, hs, st_s],
        out_shape=[jax.ShapeDtypeStruct((S, D), BF16), jax.ShapeDtypeStruct((S, DGDN), F32),
                   jax.ShapeDtypeStruct((NGH, NCH, GHD, GHD), F32)],
        input_output_aliases={8: 0}, scratch_shapes=[pltpu.VMEM((SCAN_HEADS, GHD, GHD), F32)], compiler_params=_cparams(),
    )(u, w, qd, kd, a, gl, proj, w_norm, mix)


def _gdn_scan_bwd(dmix, o, proj, w_norm, u, w, qd, kd, a, gl, states, dproj):
    def body(dy_ref, o_ref, z_ref, wn_ref, u_ref, w_ref, qd_ref, kd_ref, a_ref, gl_ref, st_ref, dproj_in,
             dz_ref, du_ref, dw_ref, dqd_ref, dkd_ref, da_ref, dgl_ref, dwn_ref, do_ref, carry_ref):
        del dproj_in
        heads = range(SCAN_HEADS_BWD)
        chunks = NCH // SCAN_PARTS_BWD

        @pl.when((pl.program_id(0) == 0) & (pl.program_id(1) == 0))
        def _():
            dwn_ref[...] = jnp.zeros_like(dwn_ref)

        @pl.when(pl.program_id(1) == 0)
        def _():
            carry_ref[...] = jnp.zeros_like(carry_ref)

        wn = wn_ref[...]
        for hh in heads:
            c = _head_cols(hh)
            ov = o_ref[:, c]
            zv = z_ref[:, c]
            g = dy_ref[:, c]
            sig = _sigmoid(zv)
            dz_ref[:, c] = (g * (ov * _rms_scale(ov) * wn) * sig * (1.0 + zv * (1.0 - sig))).astype(BF16)
            do, dwt = _rms_bwd(ov, wn, g * zv * sig)
            do_ref[:, c] = do
            dwn_ref[...] += jnp.sum(dwt, axis=0, keepdims=True)

        def step(t, dstates):
            ci = chunks - 1 - t
            rows = pl.ds(pl.multiple_of(ci * CHUNK, CHUNK), CHUNK)
            cols = [_head_cols(hh) for hh in heads]
            state = [st_ref[hh, ci] for hh in heads]
            dov = [do_ref[rows, cols[hh]] for hh in heads]
            wv = [w_ref[rows, cols[hh]] for hh in heads]
            ws = [_dot(wv[hh], state[hh]) for hh in heads]
            adov = [_dot(a_ref[hh, rows, :], dov[hh], 0, 0) for hh in heads]
            kds = [_dot(kd_ref[rows, cols[hh]], dstates[hh]) for hh in heads]
            dqd = [_dot(dov[hh], state[hh], 1, 1) for hh in heads]
            qdo = [_dot(qd_ref[rows, cols[hh]], dov[hh], 0, 0) for hh in heads]
            vn = [u_ref[rows, cols[hh]] - ws[hh] for hh in heads]
            dvn = [adov[hh] + kds[hh] for hh in heads]
            da = [_dot(dov[hh], vn[hh], 1, 1) for hh in heads]
            dkd = [_dot(vn[hh], dstates[hh], 1, 1) for hh in heads]
            dwv = [_dot(dvn[hh], state[hh], 1, 1) for hh in heads]
            wdv = [_dot(wv[hh], dvn[hh], 0, 0) for hh in heads]
            for hh in heads:
                da_ref[hh, rows, :] = da[hh]
                dqd_ref[rows, cols[hh]] = dqd[hh]
                dkd_ref[rows, cols[hh]] = dkd[hh]
                dgl = jnp.sum(jnp.sum(dstates[hh] * state[hh], axis=1, keepdims=True), axis=0, keepdims=True)
                dgl_ref[hh, ci] = jnp.broadcast_to(dgl, (1, LANES))
                du_ref[rows, cols[hh]] = dvn[hh]
                dw_ref[rows, cols[hh]] = -dwv[hh]
            return tuple(dstates[hh] * gl_ref[hh, ci] + qdo[hh] - wdv[hh] for hh in heads)

        last = lax.fori_loop(0, chunks, step, tuple(carry_ref[hh] for hh in heads))
        for hh in heads:
            carry_ref[hh] = last[hh]

    hs, a_s, gl_s, st_s, gz_s, mix_s = _scan_specs(SCAN_HEADS_BWD, SCAN_PARTS_BWD, True)
    vec = pl.BlockSpec((1, GHD), lambda g, p: (0, 0))
    tok = jax.ShapeDtypeStruct((S, DGDN), F32)
    return pl.pallas_call(
        body, name="gdn_scan_bwd", grid=(NGH // SCAN_HEADS_BWD, SCAN_PARTS_BWD),
        in_specs=[mix_s, hs, gz_s, vec, hs, hs, hs, hs, a_s, gl_s, st_s, pl.BlockSpec(memory_space=pl.ANY)],
        out_specs=[gz_s, hs, hs, hs, hs, a_s, gl_s, vec],
        out_shape=[jax.ShapeDtypeStruct((S, DPROJ_PAD), BF16), tok, tok, tok, tok,
                   jax.ShapeDtypeStruct((NGH, S, CHUNK), F32), jax.ShapeDtypeStruct((NGH, NCH, 1, LANES), F32),
                   jax.ShapeDtypeStruct((1, GHD), F32)],
        input_output_aliases={11: 0},
        scratch_shapes=[pltpu.VMEM((S // SCAN_PARTS_BWD, SCAN_HEADS_BWD * GHD), F32), pltpu.VMEM((SCAN_HEADS_BWD, GHD, GHD), F32)],
        compiler_params=_cparams(),
    )(dmix, o, proj, w_norm, u, w, qd, kd, a, gl, states, dproj)


def _place():
    return lax.axis_index("x"), lax.axis_index("y"), lax.axis_index("c")


def _other_chips(x, y):
    return [(1 - x, y), (x, 1 - y), (1 - x, 1 - y)]


HBM = pl.BlockSpec(memory_space=pltpu.HBM)
VMEM = pl.BlockSpec(memory_space=pltpu.VMEM)


def _half_rows(ref_or_rows, half):
    rows = ref_or_rows // 2
    return pl.ds(pl.multiple_of(half * rows, rows), rows)


class _Exchange:
    def __init__(self, inputs, out_shape, n_sems, start, finish):
        self.inputs, self.out_shape, self.n_sems, self.start, self.finish = inputs, out_shape, n_sems, start, finish

    def sem_shapes(self):
        return [pltpu.SemaphoreType.DMA((self.n_sems,)), pltpu.SemaphoreType.DMA((self.n_sems,))]

    def split(self, refs):
        n_in, n_out = len(self.inputs), len(self.out_shape)
        return refs[:n_in], refs[n_in:n_in + n_out], refs[n_in + n_out], refs[n_in + n_out + 1]


def _run_exchange(ex, name):
    def body(*refs):
        parts = ex.split(refs)
        ex.start(*parts)
        ex.finish(*parts)

    return pl.pallas_call(
        body, name=name, in_specs=[HBM] * len(ex.inputs), out_specs=[HBM] * len(ex.out_shape), out_shape=ex.out_shape,
        scratch_shapes=ex.sem_shapes(), compiler_params=_cparams(),
    )(*ex.inputs)


def _allgather_exchange(shards, whole=()):
    n, nw = len(shards), len(whole)

    def plan(src, outs, send_sems, recv_sems):
        x, y, c = _place()
        chips = _other_chips(x, y)
        chip_ids = [2 * ch[0] + ch[1] for ch in chips]

        def copy(a, k, chip_index, half, to, from_src):
            rows = _half_rows(src[a].shape[0], half)
            dst = outs[a].at[chip_index, rows]
            return pltpu.make_async_remote_copy(
                src_ref=src[a].at[rows] if from_src else dst, dst_ref=dst, send_sem=send_sems.at[6 * a + k],
                recv_sem=recv_sems.at[6 * a + k], device_id=to, device_id_type=MESH)

        def whole_copy(b, k, chip_index, to):
            return pltpu.make_async_remote_copy(
                src_ref=src[n + b], dst_ref=outs[n + b].at[chip_index], send_sem=send_sems.at[6 * n + 3 * b + k],
                recv_sem=recv_sems.at[6 * n + 3 * b + k], device_id=to, device_id_type=MESH)

        me, sibling = (x, y, c), (x, y, 1 - c)
        first = [copy(a, j, 2 * x + y, c, (*chips[j], c), True) for a in range(n) for j in range(3)]
        first += [whole_copy(b, j, 2 * x + y, (*chips[j], c)) for b in range(nw) for j in range(3)]
        landing = [copy(a, j, chip_ids[j], c, me, False) for a in range(n) for j in range(3)]
        passed = [copy(a, 3 + j, chip_ids[j], c, sibling, False) for a in range(n) for j in range(3)]
        arriving = [copy(a, 3 + j, chip_ids[j], 1 - c, me, False) for a in range(n) for j in range(3)]
        arriving += [whole_copy(b, j, chip_ids[j], me) for b in range(nw) for j in range(3)]
        return first, landing, passed, arriving

    def start(*refs):
        for cp in plan(*refs)[0]:
            cp.start()

    def finish(*refs):
        first, landing, passed, arriving = plan(*refs)
        for lands, onward in zip(landing, passed):
            lands.wait_recv()
            onward.start()
        for cp in arriving:
            cp.wait_recv()
        for cp in first + passed:
            cp.wait_send()

    out_shape = [jax.ShapeDtypeStruct((NCHIP,) + s.shape, s.dtype) for s in list(shards) + list(whole)]
    return _Exchange(list(shards) + list(whole), out_shape, 6 * n + 3 * nw, start, finish)


def _with_own(gathered, own):
    x, y, _ = _place()
    return lax.dynamic_update_index_in_dim(gathered, own, 2 * x + y, axis=0)


def _simple_exchange(inputs, out_shape, copies_of):
    def start(*refs):
        for cp in copies_of(*refs):
            cp.start()

    def finish(*refs):
        for cp in copies_of(*refs):
            cp.wait()

    return _Exchange(list(inputs), out_shape, len(out_shape) * 3, start, finish)


def _pair_exchange(grads):
    def copies_of(src, outs, send_sems, recv_sems):
        x, y, c = _place()
        return [pltpu.make_async_remote_copy(
            src_ref=src[a].at[:, _half_rows(src[a].shape[1], 1 - c)], dst_ref=outs[a], send_sem=send_sems.at[a],
            recv_sem=recv_sems.at[a], device_id=(x, y, 1 - c), device_id_type=MESH) for a in range(len(src))]

    return _simple_exchange(
        grads, [jax.ShapeDtypeStruct((g.shape[0], g.shape[1] // 2, g.shape[2]), g.dtype) for g in grads], copies_of)


def _pair_sum(grads, theirs, name):
    n = len(grads)

    def body(*refs):
        south = lax.axis_index("c") == 0
        for a in range(n):
            g = refs[a][...]
            half = g.shape[0] // 2
            mine = jnp.where(south, g[:half], g[half:])
            refs[2 * n + a][...] = (mine.astype(F32) + refs[n + a][...].astype(F32)).astype(BF16)

    def specs(arrs):
        return [pl.BlockSpec((None,) + g.shape[1:], lambda j: (j, 0, 0)) for g in arrs]

    return pl.pallas_call(
        body, name=name, grid=(NCHIP,), in_specs=specs(grads) + specs(theirs), out_specs=specs(theirs),
        out_shape=[jax.ShapeDtypeStruct(g.shape, BF16) for g in theirs], compiler_params=_cparams(),
    )(*grads, *theirs)


def _chip_exchange(parts):
    def copies_of(src, outs, send_sems, recv_sems):
        x, y, c = _place()
        return [pltpu.make_async_remote_copy(
            src_ref=src[a].at[2 * chip[0] + chip[1]], dst_ref=outs[a].at[k], send_sem=send_sems.at[3 * a + k],
            recv_sem=recv_sems.at[3 * a + k], device_id=(*chip, c), device_id_type=MESH)
            for a in range(len(src)) for k, chip in enumerate(_other_chips(x, y))]

    return _simple_exchange(parts, [jax.ShapeDtypeStruct((NCHIP - 1,) + p.shape[1:], p.dtype) for p in parts], copies_of)


def _chip_sum(parts, received, exchange=None):
    n = len(parts)
    steps = 4
    ex_in, ex_in_specs, ex_out_specs, ex_out_shape, ex_scratch = _hosted(exchange)

    def body(*refs):
        ex_refs = refs[2 * n:2 * n + len(ex_in)] + refs[3 * n + len(ex_in):]
        if exchange is not None:
            @pl.when(pl.program_id(0) == 0)
            def _():
                exchange.start(*exchange.split(ex_refs))

        chip = 2 * lax.axis_index("x") + lax.axis_index("y")
        for a in range(n):
            p, r = refs[a], refs[n + a]
            own = jnp.where(chip == 0, p[0], jnp.where(chip == 1, p[1], jnp.where(chip == 2, p[2], p[3])))
            refs[2 * n + len(ex_in) + a][...] = ((own.astype(F32) + r[0].astype(F32)) + r[1].astype(F32)) + r[2].astype(F32)

        if exchange is not None:
            @pl.when(pl.program_id(0) == steps - 1)
            def _():
                exchange.finish(*exchange.split(ex_refs))

    def specs(arrs):
        return [pl.BlockSpec((g.shape[0], g.shape[1] // steps, g.shape[2]), lambda i: (0, i, 0)) for g in arrs]

    out_specs = [pl.BlockSpec((g.shape[1] // steps, g.shape[2]), lambda i: (i, 0)) for g in parts]
    res = pl.pallas_call(
        body, name="grads_chip_sum", grid=(steps,), in_specs=specs(parts) + specs(received) + ex_in_specs,
        out_specs=out_specs + ex_out_specs, out_shape=[jax.ShapeDtypeStruct(g.shape[1:], F32) for g in parts] + ex_out_shape,
        scratch_shapes=ex_scratch, compiler_params=_cparams(),
    )(*parts, *received, *ex_in)
    return res[:n], res[n:]


def _pair_share(halves):
    def copies_of(src, outs, send_sems, recv_sems):
        x, y, c = _place()
        return [pltpu.make_async_remote_copy(
            src_ref=src[a], dst_ref=outs[a], send_sem=send_sems.at[a], recv_sem=recv_sems.at[a],
            device_id=(x, y, 1 - c), device_id_type=MESH) for a in range(len(src))]

    return _simple_exchange(halves, [jax.ShapeDtypeStruct(h.shape, F32) for h in halves], copies_of)


def _adamw_math(w, g, m, v):
    nm = ADAM_B1 * m + (1.0 - ADAM_B1) * g
    nv = ADAM_B2 * v + (1.0 - ADAM_B2) * jnp.square(g)
    m_hat = nm / (1.0 - ADAM_B1 ** ADAM_STEP)
    v_hat = nv / (1.0 - ADAM_B2 ** ADAM_STEP)
    return -ADAM_LR * (m_hat / (jnp.sqrt(v_hat) + ADAM_EPS) + ADAM_WD * w), nm, nv


def _adamw_big(ws, g_mine, g_theirs, ms, vs, exchange=None):
    n = len(ws)
    steps = 8
    ex_in, ex_in_specs, ex_out_specs, ex_out_shape, ex_scratch = _hosted(exchange)

    def body(*refs):
        ex_refs = refs[5 * n:5 * n + len(ex_in)] + refs[9 * n + len(ex_in):]
        outs = refs[5 * n + len(ex_in):9 * n + len(ex_in)]
        if exchange is not None:
            @pl.when(pl.program_id(0) == 0)
            def _():
                exchange.start(*exchange.split(ex_refs))

        own_half = (pl.program_id(0) // (steps // 2)) == lax.axis_index("c")
        for a in range(n):
            g = jnp.where(own_half, refs[n + a][...], refs[2 * n + a][...])
            d, nm, nv = _adamw_math(refs[a][...], g, refs[3 * n + a][...], refs[4 * n + a][...])
            outs[a][...] = g
            outs[n + a][...] = d
            outs[2 * n + a][...] = nm
            outs[3 * n + a][...] = nv

        if exchange is not None:
            @pl.when(pl.program_id(0) == steps - 1)
            def _():
                exchange.finish(*exchange.split(ex_refs))

    specs = [pl.BlockSpec((w.shape[0] // steps, w.shape[1]), lambda i: (i, 0)) for w in ws]
    half_specs = [pl.BlockSpec((g.shape[0] // (steps // 2), g.shape[1]), lambda i: (i % (steps // 2), 0)) for g in g_mine]
    shapes = [jax.ShapeDtypeStruct(w.shape, F32) for w in ws]
    res = pl.pallas_call(
        body, name="adamw_big", grid=(steps,), in_specs=specs + half_specs * 2 + specs * 2 + ex_in_specs,
        out_specs=specs * 4 + ex_out_specs, out_shape=shapes * 4 + ex_out_shape, scratch_shapes=ex_scratch,
        compiler_params=_cparams(),
    )(*ws, *g_mine, *g_theirs, *ms, *vs, *ex_in)
    return res[:n], res[n:2 * n], res[2 * n:3 * n], res[3 * n:4 * n], res[4 * n:]


def _adamw_in(w, g_mine, g_theirs, m, v):
    half = D // 2

    def body(w_ref, gm_ref, gt_ref, m_ref, v_ref, g_out, d_out, nm_out, nv_out, g_ref):
        south = lax.axis_index("c") == 0
        g_ref[0:half, :] = jnp.where(south, gm_ref[...], gt_ref[...])
        g_ref[half:D, :] = jnp.where(south, gt_ref[...], gm_ref[...])
        g = g_ref[0:CW, :]
        d, nm, nv = _adamw_math(w_ref[...], g, m_ref[...], v_ref[...])
        g_out[...] = g
        d_out[...] = d
        nm_out[...] = nm
        nv_out[...] = nv

    spec = pl.BlockSpec((CW, LANES), lambda i: (0, i))
    half_spec = pl.BlockSpec((half, LANES), lambda i: (0, i))
    return pl.pallas_call(
        body, name="adamw_in", grid=(D // LANES,), in_specs=[spec, half_spec, half_spec, spec, spec], out_specs=[spec] * 4,
        out_shape=[jax.ShapeDtypeStruct((CW, D), F32)] * 4, scratch_shapes=[pltpu.VMEM((D, LANES), F32)],
        compiler_params=_cparams(),
    )(w, g_mine, g_theirs, m, v)


NORM_NAMES = ("pre_mix_norm", "post_mix_norm", "pre_mlp_norm", "post_mlp_norm")
SMALL_NAMES = NORM_NAMES + ("gdn_conv_w", "fox_f_bias", "gdn_dt_bias", "gdn_a_log", "fox_out_norm", "gdn_out_norm")
CONV_COLS = 3 * DGDN // NCHIP


def _small_gather(d_norms, d_conv, sums, d_fox_norm, d_gdn_norm, loss_row):
    n_arrays = 6
    n_remote = n_arrays * (NDEV - 1)

    def copies_of(src, outs, send_sems, recv_sems):
        x, y, c = _place()
        me = 4 * x + 2 * y + c

        def from_me(chip_index):
            cols = pl.ds(pl.multiple_of(chip_index * CONV_COLS, LANES), CONV_COLS)
            return [src[0], src[1].at[:, cols], src[2], src[3], src[4], src[5]]

        local = [pltpu.make_async_copy(s, outs[a].at[me], send_sems.at[n_remote + a]) for a, s in enumerate(from_me(2 * x + y))]
        remote = []
        for k in range(1, NDEV):
            px, py, pc = x ^ ((k >> 2) & 1), y ^ ((k >> 1) & 1), c ^ (k & 1)
            remote += [pltpu.make_async_remote_copy(
                src_ref=s, dst_ref=outs[a].at[me], send_sem=send_sems.at[n_arrays * (k - 1) + a],
                recv_sem=recv_sems.at[n_arrays * (k - 1) + a], device_id=(px, py, pc), device_id_type=MESH)
                for a, s in enumerate(from_me(2 * px + py))]
        return local + remote

    def start(*refs):
        for cp in copies_of(*refs):
            cp.start()

    def finish(*refs):
        for cp in copies_of(*refs):
            cp.wait()

    shapes = [(4, D), (CONV_K, CONV_COLS), (8, LANES), (1, LANES), (1, LANES), (1, LANES)]
    return _Exchange([d_norms, d_conv, sums, d_fox_norm, d_gdn_norm, loss_row],
                     [jax.ShapeDtypeStruct((NDEV,) + s, F32) for s in shapes], n_remote + n_arrays, start, finish)


def _small_adamw(gathered, ws, ms, vs):
    n = len(SMALL_NAMES)
    ng = len(gathered)

    def body(*refs):
        def total(buf):
            acc = buf[0]
            for i in range(1, NDEV):
                acc = acc + buf[i]
            return acc

        t_norms, t_conv, t_sums, t_fn, t_gn, t_loss = [total(r) for r in refs[:ng]]
        w_refs, m_refs, v_refs = refs[ng:ng + n], refs[ng + n:ng + 2 * n], refs[ng + 2 * n:ng + 3 * n]
        outs = refs[ng + 3 * n:]
        outs[4 * n][...] = t_loss
        grads = [t_norms[i:i + 1, :] for i in range(4)] + [
            t_conv, t_sums[0:1, 0:NFH], t_sums[1:2, 0:NGH], t_sums[2:3, 0:NGH], t_fn[:, 0:FHD], t_gn]
        for a in range(n):
            d, nm, nv = _adamw_math(w_refs[a][...], grads[a], m_refs[a][...], v_refs[a][...])
            outs[a][...] = grads[a]
            outs[n + a][...] = d
            outs[2 * n + a][...] = nm
            outs[3 * n + a][...] = nv

    def whole(arr):
        return pl.BlockSpec(arr.shape, lambda i: (0,) * arr.ndim)

    res = pl.pallas_call(
        body, name="small_adamw", grid=(1,), in_specs=[whole(t) for t in gathered] + [whole(w) for w in ws] * 3,
        out_specs=[whole(w) for w in ws] * 4 + [pl.BlockSpec((1, LANES), lambda i: (0, 0))],
        out_shape=[jax.ShapeDtypeStruct(w.shape, F32) for w in ws] * 4 + [jax.ShapeDtypeStruct((1, LANES), F32)],
        compiler_params=_cparams(),
    )(*gathered, *ws, *ms, *vs)
    return res[:n], res[n:2 * n], res[2 * n:3 * n], res[3 * n:4 * n], res[4 * n]


CW = DPROJ // NCHIP
PROJ_RUNS = tuple((part * DFOX + hp * LANES, part * DFOX + (hp + 1) * LANES, (3 * hp + part) * LANES)
                  for hp in range(NPAIR) for part in range(3)) + (
    (1536, 1544, BLK_SMALL * LANES), (1544, 3080, BLK_GDN * LANES), (3080, 3088, BLK_SMALL * LANES + 8),
    (3088, 3600, BLK_GZ * LANES))


def _proj_pieces():
    pieces = []
    for lo, hi, at in PROJ_RUNS:
        while lo < hi:
            j = lo // CW
            end = min(hi, (j + 1) * CW)
            pieces.append((j, lo - j * CW, at, end - lo))
            at, lo = at + end - lo, end
    return pieces


RT = 256


def _to_padded_rows(gathered):
    def body(src_ref, out_ref, blocks_ref, rows_ref):
        blocks_ref[...] = src_ref[...].astype(F32)
        rows_ref[...] = jnp.zeros_like(rows_ref)
        for j, start, at, n in _proj_pieces():
            rows_ref[at:at + n, :] = blocks_ref[j, start:start + n, :]
        out_ref[...] = rows_ref[...].astype(out_ref.dtype)

    return pl.pallas_call(
        body, name="proj_rows_in", grid=(D // RT,), in_specs=[pl.BlockSpec((NCHIP, D, RT), lambda i: (0, 0, i))],
        out_specs=pl.BlockSpec((DPROJ_PAD, RT), lambda i: (0, i)), out_shape=jax.ShapeDtypeStruct((DPROJ_PAD, D), gathered.dtype),
        scratch_shapes=[pltpu.VMEM((NCHIP, D, RT), F32), pltpu.VMEM((DPROJ_PAD, RT), F32)], compiler_params=_cparams(),
    )(gathered)


def _from_padded_rows(w):
    def body(src_ref, out_ref, rows_ref, blocks_ref):
        rows_ref[...] = src_ref[...].astype(F32)
        blocks_ref[...] = jnp.zeros_like(blocks_ref)
        for j, start, at, n in _proj_pieces():
            blocks_ref[j, start:start + n, :] = rows_ref[at:at + n, :]
        out_ref[...] = blocks_ref[...].astype(out_ref.dtype)

    return pl.pallas_call(
        body, name="proj_rows_out", grid=(D // RT,), in_specs=[pl.BlockSpec((DPROJ_PAD, RT), lambda i: (0, i))],
        out_specs=pl.BlockSpec((NCHIP, D, RT), lambda i: (0, 0, i)), out_shape=jax.ShapeDtypeStruct((NCHIP, D, D), w.dtype),
        scratch_shapes=[pltpu.VMEM((DPROJ_PAD, RT), F32), pltpu.VMEM((NCHIP, D, RT), F32)], compiler_params=_cparams(),
    )(w)


def _local_step(x, target, first_weights, late_weights, reduce_late, reduce_in, pre_mix_norm, fox_f_bias, fox_out_norm,
                gdn_a_log, gdn_dt_bias, gdn_out_norm, post_mix_norm, pre_mlp_norm, post_mlp_norm):
    bias_vec = jnp.zeros((1, LANES), F32).at[0, 0:NFH].set(fox_f_bias).at[0, LANE_G:LANE_G + NGH].set(gdn_dt_bias)
    alog_vec = jnp.zeros((1, LANES), F32).at[0, LANE_G:LANE_G + NGH].set(gdn_a_log)
    w2 = jnp.concatenate([fox_out_norm, fox_out_norm], axis=1)

    h, first = _pre_norm(x, pre_mix_norm, exchange=first_weights[0])
    win_p, conv_w = first_weights[1](first)
    proj = _matmul(h, win_p, tb=True, tm=2048, tn=768, tk=1024, name="mm_proj")
    gates = _gates(proj, bias_vec, alog_vec)
    mix, fox_o, lse, late_a = _fox_fwd(proj, gates, w2, exchange=late_weights[0])
    qkv = _gdn_pre(proj, conv_w)
    (u, w, qd, kd, a_intra, gl, t_inv), late_b = _gdn_prep(qkv, gates, exchange=late_weights[1])
    wout, wup3, wdown = late_weights[2](late_a, late_b)
    mix, gdn_raw, states = _gdn_scan(u, w, qd, kd, a_intra, gl, proj, gdn_out_norm, mix)
    mixed = _matmul(mix, wout, tm=2048, tk=1024, name="mm_out")
    x1, h2 = _post_mix(x, mixed, post_mix_norm, pre_mlp_norm)

    def relu2(acc):
        r = jnp.maximum(acc, 0.0)
        return r, r * r

    up_relu, act = _matmul(h2, wup3, b3=True, tm=1024, tn=1024, tk=1024, out_dtypes=(BF16, BF16), epilogue=relu2,
                           name="mm_up")
    y = _matmul(act, wdown, tm=1024, tk=DFF, name="mm_down")
    dx2, dy, d_post_mlp, loss_row = _loss_head(x1, y, post_mlp_norm, target)

    dwdown = _matmul(act, dy, ta=True, tm=1024, tn=1024, tk=2048, out_dtypes=(BF16,), name="mm_dwdown")

    def relu2_bwd(acc, r):
        return (acc * 2.0 * r.astype(F32),)

    dup = _matmul(dy, wdown, tb=True, tm=1024, tn=1024, tk=1024, out_dtypes=(BF16,), extra=(up_relu,), epilogue=relu2_bwd,
                  name="mm_dact")
    dwup3 = _matmul(h2, dup, ta=True, tm=1024, tn=1024, tk=2048, out_dtypes=(BF16,), o3=True, name="mm_dwup")
    dh2 = _matmul(dup, wup3, tb=True, b3=True, tm=1024, tk=DFF, name="mm_dh2")
    dx1, dmixed, d_pre_mlp, d_post_mix = _mid_bwd(dh2, x1, pre_mlp_norm, dx2, mixed, post_mix_norm)
    dwout = _matmul(mix, dmixed, ta=True, tm=1024, tn=1024, tk=2048, out_dtypes=(BF16,), name="mm_dwout")
    dmix = _matmul(dmixed, wout, tb=True, tm=2048, tk=1024, name="mm_dmix")

    dfox, delta, d_fox_norm, from_sibling = _fox_norm_bwd(fox_o, dmix, w2, exchange=reduce_late[0](dwout, dwup3, dwdown))
    dproj, dcum_fox, reduced_late = _fox_bwd(proj, dfox, gates, lse, delta, exchange=reduce_late[1](from_sibling))
    dproj, du, dw, dqd, dkd, da, dgl, d_gdn_norm = _gdn_scan_bwd(dmix, gdn_raw, proj, gdn_out_norm, u, w, qd, kd,
                                                                 a_intra, gl, states, dproj)
    dqkv, dgates_gdn = _gdn_prep_bwd(qkv, gates, t_inv, du, dw, dqd, dkd, da, dgl)
    dproj, d_conv = _gdn_pre_bwd(proj, conv_w, dqkv, dproj)
    dproj, sums = _gates_bwd(proj, bias_vec, alog_vec, dgates_gdn, dcum_fox, dproj)

    dwin_p = _matmul(dproj, h, ta=True, tm=1280, tn=1024, tk=2048, out_dtypes=(BF16,), name="mm_dwin")
    exchange_in = reduce_in(dwin_p)
    dh = _matmul(dproj, win_p, tm=1024, tk=DPROJ_PAD, name="mm_dh", exchange=exchange_in)
    dh, reduced_in = dh if exchange_in is not None else (dh, [])
    grad_x, d_pre_mix = _pre_norm_bwd(dh, x, pre_mix_norm, dx1)

    d_norms = jnp.concatenate([d_pre_mix, d_post_mix, d_pre_mlp, d_post_mlp], axis=0)
    return grad_x, (d_norms, d_conv, sums, d_fox_norm, d_gdn_norm, loss_row), reduced_late, reduced_in


def kernel(x, pre_mix_norm, w_in, fox_f_bias, fox_out_norm, gdn_conv_w, gdn_a_log, gdn_dt_bias, gdn_out_norm, w_out, post_mix_norm, pre_mlp_norm, w_up, w_down, post_mlp_norm, loss_target, m_pre_mix_norm, m_w_in, m_fox_f_bias, m_fox_out_norm, m_gdn_conv_w, m_gdn_a_log, m_gdn_dt_bias, m_gdn_out_norm, m_w_out, m_post_mix_norm, m_pre_mlp_norm, m_w_up, m_w_down, m_post_mlp_norm, v_pre_mix_norm, v_w_in, v_fox_f_bias, v_fox_out_norm, v_gdn_conv_w, v_gdn_a_log, v_gdn_dt_bias, v_gdn_out_norm, v_w_out, v_post_mix_norm, v_pre_mlp_norm, v_w_up, v_w_down, v_post_mlp_norm):
    weights = dict(pre_mix_norm=pre_mix_norm, w_in=w_in, fox_f_bias=fox_f_bias, fox_out_norm=fox_out_norm, gdn_conv_w=gdn_conv_w,
                   gdn_a_log=gdn_a_log, gdn_dt_bias=gdn_dt_bias, gdn_out_norm=gdn_out_norm, w_out=w_out, post_mix_norm=post_mix_norm,
                   pre_mlp_norm=pre_mlp_norm, w_up=w_up, w_down=w_down, post_mlp_norm=post_mlp_norm)
    m_in = dict(pre_mix_norm=m_pre_mix_norm, w_in=m_w_in, fox_f_bias=m_fox_f_bias, fox_out_norm=m_fox_out_norm, gdn_conv_w=m_gdn_conv_w,
                gdn_a_log=m_gdn_a_log, gdn_dt_bias=m_gdn_dt_bias, gdn_out_norm=m_gdn_out_norm, w_out=m_w_out, post_mix_norm=m_post_mix_norm,
                pre_mlp_norm=m_pre_mlp_norm, w_up=m_w_up, w_down=m_w_down, post_mlp_norm=m_post_mlp_norm)
    v_in = dict(pre_mix_norm=v_pre_mix_norm, w_in=v_w_in, fox_f_bias=v_fox_f_bias, fox_out_norm=v_fox_out_norm, gdn_conv_w=v_gdn_conv_w,
                gdn_a_log=v_gdn_a_log, gdn_dt_bias=v_gdn_dt_bias, gdn_out_norm=v_gdn_out_norm, w_out=v_w_out, post_mix_norm=v_post_mix_norm,
                pre_mlp_norm=v_pre_mlp_norm, w_up=v_w_up, w_down=v_w_down, post_mlp_norm=v_post_mlp_norm)
    order_w = ("pre_mix_norm", "w_in", "fox_f_bias", "fox_out_norm", "gdn_conv_w", "gdn_a_log", "gdn_dt_bias", "gdn_out_norm", "w_out",
               "post_mix_norm", "pre_mlp_norm", "w_up", "w_down", "post_mlp_norm")
    big = ("w_in", "w_out", "w_up", "w_down")

    def row(v):
        return v if v.ndim == 2 else v.reshape(1, -1)

    win_shard = jnp.pad(w_in.T.astype(BF16), ((0, D - CW), (0, 0)))

    def resolve_first(gathered):
        win_g, conv_g = gathered
        return (_to_padded_rows(_with_own(win_g, win_shard)),
                _with_own(conv_g, gdn_conv_w).transpose(1, 0, 2).reshape(CONV_K, 3 * DGDN))

    late_shards = [weights[n].astype(BF16) for n in big[1:]]

    def resolve_late(gathered_a, gathered_b):
        wup3, wdown_g, wout_g = [_with_own(g, own) for g, own in
                                 zip(list(gathered_a) + list(gathered_b), late_shards[1:] + late_shards[:1])]
        return wout_g.reshape(D, D), wup3, wdown_g.reshape(DFF, D)

    pair_sums, late_blocks = {}, []

    def chip_exchange_of(names, blocks, theirs):
        for n, s in zip(names, _pair_sum(blocks, theirs, "grads_pair_sum_" + names[0])):
            pair_sums[n] = s
        return _chip_exchange([pair_sums[n] for n in names])

    def late_pair_exchange(dwout, dwup3, dwdown):
        late_blocks.extend([dwout.reshape(NCHIP, D // NCHIP, D), dwup3, dwdown.reshape(NCHIP, DFF // NCHIP, D)])
        return _pair_exchange(late_blocks)

    def late_chip_exchange(theirs):
        return chip_exchange_of(big[1:], late_blocks, theirs)

    def reduce_in(dwin_p):
        blocks = [_from_padded_rows(dwin_p)]
        return chip_exchange_of(big[:1], blocks, _run_exchange(_pair_exchange(blocks), "grads_pair_exchange_w_in"))

    grad_x, small, received_late, received_in = _local_step(
        x[0], loss_target[0], (_allgather_exchange([win_shard], whole=[gdn_conv_w]), resolve_first),
        (_allgather_exchange(late_shards[1:]), _allgather_exchange(late_shards[:1]), resolve_late),
        (late_pair_exchange, late_chip_exchange), reduce_in, row(pre_mix_norm), fox_f_bias, row(fox_out_norm),
        gdn_a_log, gdn_dt_bias,
        row(gdn_out_norm), row(post_mix_norm), row(pre_mlp_norm), row(post_mlp_norm))

    g_mine, small_gathered = _chip_sum([pair_sums[n] for n in big], list(received_in) + list(received_late),
                                       exchange=_small_gather(*small))
    g_theirs = _run_exchange(_pair_share(g_mine), "grads_pair_share")

    g_big, d_big, nm_big, nv_big, _ = _adamw_big(
        [weights[n] for n in big[1:]], g_mine[1:], g_theirs[1:], [m_in[n] for n in big[1:]], [v_in[n] for n in big[1:]])
    in_t = _adamw_in(w_in.T, g_mine[0], g_theirs[0], m_w_in.T, v_w_in.T)
    g_small, d_small, nm_small, nv_small, loss_total = _small_adamw(
        small_gathered, [row(weights[n]) for n in SMALL_NAMES], [row(m_in[n]) for n in SMALL_NAMES],
        [row(v_in[n]) for n in SMALL_NAMES])

    grads, delta, new_m, new_v = {}, {}, {}, {}
    grads["w_in"], delta["w_in"], new_m["w_in"], new_v["w_in"] = [t.T for t in in_t]
    for i, n in enumerate(big[1:]):
        grads[n], delta[n], new_m[n], new_v[n] = g_big[i], d_big[i], nm_big[i], nv_big[i]
    for i, n in enumerate(SMALL_NAMES):
        shape = weights[n].shape
        grads[n], delta[n], new_m[n], new_v[n] = (g_small[i].reshape(shape), d_small[i].reshape(shape),
                                                  nm_small[i].reshape(shape), nv_small[i].reshape(shape))
    return (loss_total[0, 0], grad_x[None], *[grads[n] for n in order_w], *[delta[n] for n in order_w], *[new_m[n] for n in order_w],
            *[new_v[n] for n in order_w])
```

```python
import jax
import jax.numpy as jnp
from jax import lax
from jax.experimental import pallas as pl
from jax.experimental.pallas import tpu as pltpu

F32 = jnp.float32
BF16 = jnp.bfloat16
MESH = pl.DeviceIdType.MESH

S = 2048
D = 1024
NFH, FHD = 8, 64
NPAIR = NFH // 2
NGH, GHD = 4, 128
DFOX = NFH * FHD
DGDN = NGH * GHD
CHUNK = 64
NCH = S // CHUNK
CONV_K = 4
DFF = 4 * D
EPS = 1e-6
DPROJ = 3600
LANES = 128
DPROJ_PAD = 3840
BLK_GDN = 12
BLK_GZ = 24
BLK_SMALL = 28
NCHIP = 4
NDEV = 8
VMEM_LIMIT = 56 * 1024 * 1024

ADAM_LR = 0.001
ADAM_B1 = 0.9
ADAM_B2 = 0.999
ADAM_EPS = 1e-08
ADAM_WD = 0.01
ADAM_STEP = 10


def _cparams(**kw):
    return pltpu.CompilerParams(vmem_limit_bytes=VMEM_LIMIT, **kw)


def _dn(ca, cb):
    return (((ca,), (cb,)), ((), ()))


def _dot(a, b, ca=1, cb=0):
    return lax.dot_general(a.astype(BF16), b.astype(BF16), _dn(ca, cb), preferred_element_type=F32)


def _hdot(a, b, ca=1, cb=0):
    return lax.dot_general(a.astype(F32), b.astype(F32), _dn(ca, cb), precision=lax.Precision.HIGHEST,
                           preferred_element_type=F32)


def _dot3(a, b, ca=1, cb=0):
    a_hi, b_hi = a.astype(BF16), b.astype(BF16)
    a_lo, b_lo = (a - a_hi.astype(F32)).astype(BF16), (b - b_hi.astype(F32)).astype(BF16)
    dn = _dn(ca, cb)
    return (lax.dot_general(a_hi, b_hi, dn, preferred_element_type=F32)
            + (lax.dot_general(a_hi, b_lo, dn, preferred_element_type=F32)
               + lax.dot_general(a_lo, b_hi, dn, preferred_element_type=F32)))


@jax.custom_vjp
def _mm_nn(a, b):
    return _dot(a, b, 1, 0)


def _mm_nn_fwd(a, b):
    return _dot(a, b, 1, 0), (a, b)


def _mm_nn_bwd(res, g):
    a, b = res
    return _dot(g, b, 1, 1), _dot(a, g, 0, 0)


_mm_nn.defvjp(_mm_nn_fwd, _mm_nn_bwd)


@jax.custom_vjp
def _mm_nt(a, b):
    return _dot(a, b, 1, 1)


def _mm_nt_fwd(a, b):
    return _dot(a, b, 1, 1), (a, b)


def _mm_nt_bwd(res, g):
    a, b = res
    return _dot(g, b, 1, 0), _dot(g, a, 0, 0)


_mm_nt.defvjp(_mm_nt_fwd, _mm_nt_bwd)


@jax.custom_vjp
def _saved_inverse(m, t_inv):
    del m
    return t_inv


def _saved_inverse_fwd(m, t_inv):
    del m
    return t_inv, t_inv


def _saved_inverse_bwd(t_inv, g):
    return -_dot3(_dot3(t_inv, g, 0, 0), t_inv, 1, 1), jnp.zeros_like(t_inv)


_saved_inverse.defvjp(_saved_inverse_fwd, _saved_inverse_bwd)


def _sigmoid(z):
    return 1.0 / (1.0 + jnp.exp(-z))


def _softplus(z):
    return jnp.maximum(z, 0.0) + jnp.log(1.0 + jnp.exp(-jnp.abs(z)))


def _silu(z):
    return z * _sigmoid(z)


def _rms_scale(x):
    return lax.rsqrt(jnp.mean(x * x, axis=-1, keepdims=True) + EPS)


def _rms_bwd(x, w, g):
    r = _rms_scale(x)
    gw = g * w
    dx = r * gw - x * (r * r * r) * jnp.mean(gw * x, axis=-1, keepdims=True)
    return dx, g * x * r


def _matmul(a, b, *, name, ta=False, tb=False, tm=512, tn=512, tk=512, out_dtypes=(F32,), b3=False, o3=False,
            extra=(), epilogue=None, exchange=None):
    m, k = (a.shape[1], a.shape[0]) if ta else a.shape
    if b3:
        n = b.shape[1] if tb else b.shape[0] * b.shape[2]
        kb = b.shape[0] * b.shape[2] if tb else b.shape[1]
    else:
        n, kb = (b.shape[0], b.shape[1]) if tb else (b.shape[1], b.shape[0])
    assert kb == k, (name, kb, k)
    tm, tn, tk = min(tm, m), min(tn, n), min(tk, k)
    assert m % tm == 0 and n % tn == 0 and k % tk == 0, (name, m, n, k, tm, tn, tk)
    nk = k // tk
    whole_k_blocks = b3 and tb and not ta and nk == 1 and b.shape[0] > 1
    n_extra = len(extra)
    n_out = len(out_dtypes)
    grid = (m // tm, n // tn, nk)
    ex_in, ex_in_specs, ex_out_specs, ex_out_shape, ex_scratch = _hosted(exchange)

    def body(*refs):
        a_ref, b_ref = refs[0], refs[1]
        extra_refs = refs[2:2 + n_extra]
        first_out = 2 + n_extra + len(ex_in)
        out_refs = refs[first_out:first_out + n_out]
        ex_refs = refs[2 + n_extra:first_out] + refs[first_out + n_out:first_out + n_out + len(ex_out_shape)] + refs[-2:]
        step = [pl.program_id(d) for d in range(3)]

        if exchange is not None:
            @pl.when((step[0] == 0) & (step[1] == 0) & (step[2] == 0))
            def _():
                exchange.start(*exchange.split(ex_refs))

        def finish(acc):
            outs = (acc,) if epilogue is None else epilogue(acc, *[r[...] for r in extra_refs])
            for o_ref, val in zip(out_refs, outs):
                o_ref[...] = val.astype(o_ref.dtype)

        if whole_k_blocks:
            width = b.shape[2]
            part = _dot(a_ref[:, 0:width], b_ref[0], 1, 1)
            for blk in range(1, b.shape[0]):
                part = part + _dot(a_ref[:, blk * width:(blk + 1) * width], b_ref[blk], 1, 1)
        else:
            part = _dot(a_ref[...], b_ref[...], 0 if ta else 1, 1 if tb else 0)
        if nk == 1:
            finish(part)
        else:
            acc_ref = refs[first_out + n_out + len(ex_out_shape)]

            @pl.when(step[2] == 0)
            def _():
                acc_ref[...] = part

            @pl.when(step[2] > 0)
            def _():
                acc_ref[...] += part

            @pl.when(step[2] == nk - 1)
            def _():
                finish(acc_ref[...])

        if exchange is not None:
            @pl.when((step[0] == grid[0] - 1) & (step[1] == grid[1] - 1) & (step[2] == nk - 1))
            def _():
                exchange.finish(*exchange.split(ex_refs))

    a_spec = pl.BlockSpec((tk, tm), lambda i, j, kk: (kk, i)) if ta else pl.BlockSpec((tm, tk), lambda i, j, kk: (i, kk))
    if whole_k_blocks:
        b_spec = pl.BlockSpec((b.shape[0], tn, b.shape[2]), lambda i, j, kk: (0, j, 0))
    elif b3 and tb:
        assert b.shape[2] == tk
        b_spec = pl.BlockSpec((None, tn, tk), lambda i, j, kk: (kk, j, 0))
    elif b3:
        assert b.shape[2] == tn
        b_spec = pl.BlockSpec((None, tk, tn), lambda i, j, kk: (j, kk, 0))
    elif tb:
        b_spec = pl.BlockSpec((tn, tk), lambda i, j, kk: (j, kk))
    else:
        b_spec = pl.BlockSpec((tk, tn), lambda i, j, kk: (kk, j))
    tile = pl.BlockSpec((tm, tn), lambda i, j, kk: (i, j))
    out_specs = [tile] * n_out
    out_shape = [jax.ShapeDtypeStruct((m, n), dt) for dt in out_dtypes]
    if o3:
        out_specs[0] = pl.BlockSpec((None, tm, tn), lambda i, j, kk: (j, i, 0))
        out_shape[0] = jax.ShapeDtypeStruct((n // tn, m, tn), out_dtypes[0])
    res = pl.pallas_call(
        body, name=name, grid=grid,
        in_specs=[a_spec, b_spec] + [tile] * n_extra + ex_in_specs, out_specs=out_specs + ex_out_specs,
        out_shape=out_shape + ex_out_shape,
        scratch_shapes=([pltpu.VMEM((tm, tn), F32)] if nk > 1 else []) + ex_scratch,
        compiler_params=_cparams(),
    )(a, b, *extra, *ex_in)
    if exchange is not None:
        return (res[0] if n_out == 1 else res[:n_out]), res[n_out:]
    return res[0] if n_out == 1 else res


TR = 256


def _row_spec(cols):
    return pl.BlockSpec((TR, cols), lambda i: (i, 0))


def _vec_spec(cols):
    return pl.BlockSpec((1, cols), lambda i: (0, 0))


def _pre_norm(x, w, exchange=None):
    ex_in, ex_in_specs, ex_out_specs, ex_out_shape, ex_scratch = _hosted(exchange)

    def body(*refs):
        x_ref, w_ref, h_ref = refs[0], refs[1], refs[2 + len(ex_in)]
        ex_refs = refs[2:2 + len(ex_in)] + refs[3 + len(ex_in):]
        if exchange is not None:
            @pl.when(pl.program_id(0) == 0)
            def _():
                exchange.start(*exchange.split(ex_refs))

        xv = x_ref[...]
        h_ref[...] = (xv * _rms_scale(xv) * w_ref[...]).astype(BF16)

        if exchange is not None:
            @pl.when(pl.program_id(0) == S // TR - 1)
            def _():
                exchange.finish(*exchange.split(ex_refs))

    res = pl.pallas_call(
        body, name="pre_norm", grid=(S // TR,), in_specs=[_row_spec(D), _vec_spec(D)] + ex_in_specs,
        out_specs=[_row_spec(D)] + ex_out_specs, out_shape=[jax.ShapeDtypeStruct((S, D), BF16)] + ex_out_shape,
        scratch_shapes=ex_scratch, compiler_params=_cparams(),
    )(x, w, *ex_in)
    return res[0], res[1:]


def _post_mix(x, mixed, w_post, w_pre_mlp):
    def body(x_ref, m_ref, wp_ref, wm_ref, x1_ref, h2_ref):
        mv = m_ref[...]
        x1 = x_ref[...] + mv * _rms_scale(mv) * wp_ref[...]
        x1_ref[...] = x1
        h2_ref[...] = (x1 * _rms_scale(x1) * wm_ref[...]).astype(BF16)

    return pl.pallas_call(
        body, name="post_mix", grid=(S // TR,),
        in_specs=[_row_spec(D), _row_spec(D), _vec_spec(D), _vec_spec(D)], out_specs=[_row_spec(D), _row_spec(D)],
        out_shape=[jax.ShapeDtypeStruct((S, D), F32), jax.ShapeDtypeStruct((S, D), BF16)], compiler_params=_cparams(),
    )(x, mixed, w_post, w_pre_mlp)


def _loss_head(x1, y, w_post_mlp, target):
    def body(x1_ref, y_ref, w_ref, t_ref, dx2_ref, dy_ref, dw_ref, loss_ref):
        i = pl.program_id(0)
        yv = y_ref[...]
        w = w_ref[...]
        x2 = x1_ref[...] + yv * _rms_scale(yv) * w
        err = x2 - t_ref[...]
        dx2 = err * (1.0 / D)
        dx2_ref[...] = dx2
        dy, dwt = _rms_bwd(yv, w, dx2)
        dy_ref[...] = dy.astype(BF16)

        @pl.when(i == 0)
        def _():
            dw_ref[...] = jnp.zeros_like(dw_ref)
            loss_ref[...] = jnp.zeros_like(loss_ref)

        dw_ref[...] += jnp.sum(dwt, axis=0, keepdims=True)
        part = 0.5 * jnp.sum(jnp.mean(err * err, axis=-1, keepdims=True), axis=0, keepdims=True)
        loss_ref[...] += jnp.broadcast_to(part, loss_ref.shape)

    return pl.pallas_call(
        body, name="loss_head", grid=(S // TR,),
        in_specs=[_row_spec(D), _row_spec(D), _vec_spec(D), _row_spec(D)],
        out_specs=[_row_spec(D), _row_spec(D), _vec_spec(D), _vec_spec(LANES)],
        out_shape=[jax.ShapeDtypeStruct((S, D), F32), jax.ShapeDtypeStruct((S, D), BF16),
                   jax.ShapeDtypeStruct((1, D), F32), jax.ShapeDtypeStruct((1, LANES), F32)],
        compiler_params=_cparams(),
    )(x1, y, w_post_mlp, target)


def _mid_bwd(dh2, x1, w_pre_mlp, dx2, mixed, w_post):
    def body(dh2_ref, x1_ref, wm_ref, dx2_ref, m_ref, wp_ref, dx1_ref, dm_ref, dwm_ref, dwp_ref):
        i = pl.program_id(0)
        dxa, dwm = _rms_bwd(x1_ref[...], wm_ref[...], dh2_ref[...])
        dx1 = dx2_ref[...] + dxa
        dx1_ref[...] = dx1
        dm, dwp = _rms_bwd(m_ref[...], wp_ref[...], dx1)
        dm_ref[...] = dm.astype(BF16)

        @pl.when(i == 0)
        def _():
            dwm_ref[...] = jnp.zeros_like(dwm_ref)
            dwp_ref[...] = jnp.zeros_like(dwp_ref)

        dwm_ref[...] += jnp.sum(dwm, axis=0, keepdims=True)
        dwp_ref[...] += jnp.sum(dwp, axis=0, keepdims=True)

    return pl.pallas_call(
        body, name="mid_bwd", grid=(S // TR,),
        in_specs=[_row_spec(D), _row_spec(D), _vec_spec(D), _row_spec(D), _row_spec(D), _vec_spec(D)],
        out_specs=[_row_spec(D), _row_spec(D), _vec_spec(D), _vec_spec(D)],
        out_shape=[jax.ShapeDtypeStruct((S, D), F32), jax.ShapeDtypeStruct((S, D), BF16),
                   jax.ShapeDtypeStruct((1, D), F32), jax.ShapeDtypeStruct((1, D), F32)],
        compiler_params=_cparams(),
    )(dh2, x1, w_pre_mlp, dx2, mixed, w_post)


def _pre_norm_bwd(dh, x, w, dx1):
    def body(dh_ref, x_ref, w_ref, dx1_ref, dx_ref, dw_ref):
        i = pl.program_id(0)
        dxa, dwt = _rms_bwd(x_ref[...], w_ref[...], dh_ref[...])
        dx_ref[...] = dx1_ref[...] + dxa

        @pl.when(i == 0)
        def _():
            dw_ref[...] = jnp.zeros_like(dw_ref)

        dw_ref[...] += jnp.sum(dwt, axis=0, keepdims=True)

    return pl.pallas_call(
        body, name="pre_norm_bwd", grid=(S // TR,),
        in_specs=[_row_spec(D), _row_spec(D), _vec_spec(D), _row_spec(D)], out_specs=[_row_spec(D), _vec_spec(D)],
        out_shape=[jax.ShapeDtypeStruct((S, D), F32), jax.ShapeDtypeStruct((1, D), F32)], compiler_params=_cparams(),
    )(dh, x, w, dx1)


BQ = 512
NQ = S // BQ
LANE_BETA, LANE_G = 8, 12


def _gate_lanes(shape):
    lane = lax.broadcasted_iota(jnp.int32, shape, 1)
    return lane < LANE_BETA, (lane >= LANE_BETA) & (lane < LANE_G), (lane >= LANE_G) & (lane < LANE_G + NGH)


def _gates(proj, bias_vec, alog_vec):
    def body(s_ref, b_ref, a_ref, o_ref, carry_ref):
        i = pl.program_id(0)

        @pl.when(i == 0)
        def _():
            carry_ref[...] = jnp.zeros_like(carry_ref)

        z = s_ref[...] + b_ref[...]
        tail = jnp.log(1.0 + jnp.exp(-jnp.abs(z)))
        sp = jnp.maximum(z, 0.0) + tail
        lf = jnp.minimum(z, 0.0) - tail
        r = lax.broadcasted_iota(jnp.int32, (BQ, BQ), 0)
        c = lax.broadcasted_iota(jnp.int32, (BQ, BQ), 1)
        tri = (c <= r).astype(F32)
        cum = _hdot(tri, lf) + carry_ref[...]
        carry_ref[...] = cum[BQ - 1:BQ, :]
        is_fox, is_beta, is_g = _gate_lanes(z.shape)
        o_ref[...] = jnp.where(is_fox, cum, jnp.where(is_beta, _sigmoid(z), jnp.where(is_g, -jnp.exp(a_ref[...]) * sp, 0.0)))

    return pl.pallas_call(
        body, name="gates", grid=(NQ,),
        in_specs=[pl.BlockSpec((BQ, LANES), lambda i: (i, BLK_SMALL)), _vec_spec(LANES), _vec_spec(LANES)],
        out_specs=pl.BlockSpec((BQ, LANES), lambda i: (i, 0)), out_shape=jax.ShapeDtypeStruct((S, LANES), F32),
        scratch_shapes=[pltpu.VMEM((1, LANES), F32)], compiler_params=_cparams(),
    )(proj, bias_vec, alog_vec)


def _gates_bwd(proj, bias_vec, alog_vec, dgates_gdn, dcum_fox, dproj):
    def body(s_ref, b_ref, a_ref, dg_ref, dc_ref, dproj_in, dproj_ref, red_ref, carry_ref):
        del dproj_in
        i = pl.program_id(0)

        @pl.when(i == 0)
        def _():
            carry_ref[...] = jnp.zeros_like(carry_ref)
            red_ref[...] = jnp.zeros_like(red_ref)

        z = s_ref[...] + b_ref[...]
        dg = dg_ref[...] + dc_ref[...]
        r = lax.broadcasted_iota(jnp.int32, (BQ, BQ), 0)
        c = lax.broadcasted_iota(jnp.int32, (BQ, BQ), 1)
        upper = (c >= r).astype(F32)
        dlf = _hdot(upper, dg) + carry_ref[...]
        carry_ref[...] = dlf[0:1, :]
        sig = _sigmoid(z)
        g_scale = -jnp.exp(a_ref[...])
        is_fox, is_beta, is_g = _gate_lanes(z.shape)
        ds = jnp.where(is_fox, dlf * (1.0 - sig), jnp.where(is_beta, dg * sig * (1.0 - sig), jnp.where(is_g, dg * g_scale * sig, 0.0)))
        dproj_ref[:, 0:LANES] = ds.astype(BF16)
        dproj_ref[:, LANES:2 * LANES] = jnp.zeros((BQ, LANES), BF16)
        dalog = jnp.where(is_g, dg * g_scale * _softplus(z), 0.0)
        sums = jnp.sum(ds, axis=0, keepdims=True)
        red_ref[0:1, :] += jnp.where(is_fox[0:1], sums, 0.0)
        red_ref[1:2, :] += pltpu.roll(jnp.where(is_g[0:1], sums, 0.0), LANES - LANE_G, 1)
        red_ref[2:3, :] += pltpu.roll(jnp.sum(dalog, axis=0, keepdims=True), LANES - LANE_G, 1)

    blk = pl.BlockSpec((BQ, LANES), lambda i: (NQ - 1 - i, 0))
    return pl.pallas_call(
        body, name="gates_bwd", grid=(NQ,),
        in_specs=[pl.BlockSpec((BQ, LANES), lambda i: (NQ - 1 - i, BLK_SMALL)), _vec_spec(LANES), _vec_spec(LANES), blk, blk,
                  pl.BlockSpec(memory_space=pl.ANY)],
        out_specs=[pl.BlockSpec((BQ, 2 * LANES), lambda i: (NQ - 1 - i, BLK_SMALL // 2)), pl.BlockSpec((8, LANES), lambda i: (0, 0))],
        out_shape=[jax.ShapeDtypeStruct((S, DPROJ_PAD), BF16), jax.ShapeDtypeStruct((8, LANES), F32)],
        input_output_aliases={5: 0},
        scratch_shapes=[pltpu.VMEM((1, LANES), F32)], compiler_params=_cparams(),
    )(proj, bias_vec, alog_vec, dgates_gdn, dcum_fox, dproj)


FOX_SCALE = FHD ** -0.5
FOX_PAIRS = 2
FOX_PAIRS_BWD = 2


def _head_mask(e):
    lane = lax.broadcasted_iota(jnp.int32, (1, LANES), 1)
    return (lane >= e * FHD) & (lane < (e + 1) * FHD)


def _lane_col(vals, index):
    lane = lax.broadcasted_iota(jnp.int32, vals.shape, 1)
    return jnp.sum(jnp.where(lane == index, vals, 0.0), axis=1, keepdims=True)


def _sublane_row(vals, index):
    row = lax.broadcasted_iota(jnp.int32, vals.shape, 0)
    return jnp.sum(jnp.where(row == index, vals, 0.0), axis=0, keepdims=True)


def _pair_cols(c0, c1):
    lane = lax.broadcasted_iota(jnp.int32, (c0.shape[0], 2), 1)
    return jnp.where(lane == 0, c0, c1)


def _split3(x):
    hi = x.astype(BF16).astype(F32)
    rest = x - hi
    mid = rest.astype(BF16).astype(F32)
    return hi, mid, (rest - mid).astype(BF16).astype(F32)


def _fox_operand(vals, e, cum, is_query):
    lane = lax.broadcasted_iota(jnp.int32, (1, LANES), 1)
    base = (1 - e) * FHD
    parts = _split3(cum)
    own = jnp.where(_head_mask(e), vals * FOX_SCALE if is_query else vals, 0.0)
    cum_at, ones_at = (base, base + 3) if is_query else (base + 3, base)
    sign = 1.0 if is_query else -1.0
    out = own + jnp.where((lane >= ones_at) & (lane < ones_at + 3), 1.0, 0.0)
    for i, part in enumerate(parts):
        out = out + jnp.where(lane == cum_at + i, sign * part, 0.0)
    return out.astype(BF16)


def _causal_block():
    return lax.broadcasted_iota(jnp.int32, (BQ, BQ), 1) <= lax.broadcasted_iota(jnp.int32, (BQ, BQ), 0)


def _head_rms(o, masks):
    o2 = o * o
    r = [lax.rsqrt(jnp.sum(jnp.where(mk, o2, 0.0), axis=1, keepdims=True) * (1.0 / FHD) + EPS) for mk in masks]
    return jnp.where(masks[0], r[0], r[1])


def _hosted(exchange):
    if exchange is None:
        return [], [], [], [], []
    return (exchange.inputs, [HBM] * len(exchange.inputs), [HBM] * len(exchange.out_shape), exchange.out_shape,
            exchange.sem_shapes())


def _fox_fwd(proj, gates, w2, exchange=None):
    ex_in, ex_in_specs, ex_out_specs, ex_out_shape, ex_scratch = _hosted(exchange)

    n_in = 3 * FOX_PAIRS + 2
    heads = [(pp, e) for pp in range(FOX_PAIRS) for e in range(2)]

    def body(*refs):
        qkv_refs, g_ref, w_ref = refs[:3 * FOX_PAIRS], refs[3 * FOX_PAIRS], refs[3 * FOX_PAIRS + 1]
        mix_ref, o_ref, lse_ref = refs[n_in + len(ex_in):n_in + 3 + len(ex_in)]
        ka_ref, vb_ref = refs[n_in + 3 + len(ex_in) + len(ex_out_shape):n_in + 5 + len(ex_in) + len(ex_out_shape)]
        ex_refs = refs[n_in:n_in + len(ex_in)] + refs[n_in + 3 + len(ex_in):n_in + 3 + len(ex_in) + len(ex_out_shape)] + refs[-2:]
        grp, qi = pl.program_id(0), pl.program_id(1)

        def head_index(pp, e):
            return 2 * (FOX_PAIRS * grp + pp) + e

        if exchange is not None:
            @pl.when((grp == 0) & (qi == 0))
            def _():
                exchange.start(*exchange.split(ex_refs))

        @pl.when(qi == 0)
        def _():
            gt = g_ref[...]
            for pp in range(FOX_PAIRS):
                kv = qkv_refs[3 * pp + 1][...]
                for e in range(2):
                    ka_ref[2 * pp + e] = _fox_operand(kv, e, _lane_col(gt, head_index(pp, e)), False)
                vb_ref[pp] = qkv_refs[3 * pp + 2][...].astype(BF16)

        masks = [_head_mask(0), _head_mask(1)]
        gt = g_ref[pl.ds(pl.multiple_of(qi * BQ, BQ), BQ), :]
        qs = [_fox_operand(qkv_refs[3 * pp][...], e, _lane_col(gt, head_index(pp, e)), True) for pp, e in heads]
        n = range(len(heads))

        def block(kj, carry, diagonal):
            rows = pl.ds(pl.multiple_of(kj * BQ, BQ), BQ)
            s = [_dot(qs[i], ka_ref[i, rows, :], 1, 1) for i in n]
            if diagonal:
                s = [jnp.where(_causal_block(), s[i], -jnp.inf) for i in n]
            m_new = [jnp.maximum(carry[i][0], jnp.max(s[i], axis=-1, keepdims=True)) for i in n]
            p = [jnp.exp(s[i] - m_new[i]) for i in n]
            alpha = [jnp.exp(carry[i][0] - m_new[i]) for i in n]
            l_new = [alpha[i] * carry[i][1] + jnp.sum(p[i], axis=-1, keepdims=True) for i in n]
            pv = [_dot(p[i], vb_ref[heads[i][0], rows, :]) for i in n]
            return tuple((m_new[i], l_new[i], alpha[i] * carry[i][2] + pv[i]) for i in n)

        one = (jnp.full((BQ, 1), -jnp.inf, F32), jnp.zeros((BQ, 1), F32), jnp.zeros((BQ, LANES), F32))
        below = lax.fori_loop(0, qi, lambda kj, carry: block(kj, carry, False), (one,) * len(heads))
        done = block(qi, below, True)
        for pp in range(FOX_PAIRS):
            (m0, l0, a0), (m1, l1, a1) = done[2 * pp], done[2 * pp + 1]
            o = jnp.where(masks[0], a0 / l0, a1 / l1)
            cols = slice(pp * LANES, (pp + 1) * LANES)
            o_ref[:, cols] = o
            mix_ref[:, cols] = (o * _head_rms(o, masks) * w_ref[...]).astype(BF16)
            lse_ref[pp] = _pair_cols(m0 + jnp.log(l0), m1 + jnp.log(l1))

        if exchange is not None:
            @pl.when((grp == NPAIR // FOX_PAIRS - 1) & (qi == NQ - 1))
            def _():
                exchange.finish(*exchange.split(ex_refs))

    qkv_specs = []
    for pp in range(FOX_PAIRS):
        qkv_specs.append(pl.BlockSpec((BQ, LANES), lambda g, i, pp=pp: (i, 3 * (FOX_PAIRS * g + pp))))
        qkv_specs.append(pl.BlockSpec((S, LANES), lambda g, i, pp=pp: (0, 3 * (FOX_PAIRS * g + pp) + 1)))
        qkv_specs.append(pl.BlockSpec((S, LANES), lambda g, i, pp=pp: (0, 3 * (FOX_PAIRS * g + pp) + 2)))
    blk = pl.BlockSpec((BQ, FOX_PAIRS * LANES), lambda g, i: (i, g))
    res = pl.pallas_call(
        body, name="fox_fwd", grid=(NPAIR // FOX_PAIRS, NQ),
        in_specs=qkv_specs + [pl.BlockSpec((S, LANES), lambda g, i: (0, 0)), pl.BlockSpec((1, LANES), lambda g, i: (0, 0))]
        + ex_in_specs,
        out_specs=[blk, blk, pl.BlockSpec((FOX_PAIRS, BQ, 2), lambda g, i: (g, i, 0))] + ex_out_specs,
        out_shape=[jax.ShapeDtypeStruct((S, D), BF16), jax.ShapeDtypeStruct((S, DFOX), F32),
                   jax.ShapeDtypeStruct((NPAIR, S, 2), F32)] + ex_out_shape,
        scratch_shapes=[pltpu.VMEM((2 * FOX_PAIRS, S, LANES), BF16), pltpu.VMEM((FOX_PAIRS, S, LANES), BF16)] + ex_scratch,
        compiler_params=_cparams(),
    )(*([proj] * (3 * FOX_PAIRS)), gates, w2, *ex_in)
    return res[0], res[1], res[2], res[3:]


def _fox_norm_bwd(o, dmix, w2, exchange=None):
    ex_in, ex_in_specs, ex_out_specs, ex_out_shape, ex_scratch = _hosted(exchange)

    def body(*refs):
        o_ref, g_ref, w_ref = refs[:3]
        do_ref, dl_ref, dw_ref = refs[3 + len(ex_in):6 + len(ex_in)]
        ex_refs = refs[3:3 + len(ex_in)] + refs[6 + len(ex_in):]
        hp, qi = pl.program_id(0), pl.program_id(1)

        if exchange is not None:
            @pl.when((hp == 0) & (qi == 0))
            def _():
                exchange.start(*exchange.split(ex_refs))

        masks = [_head_mask(0), _head_mask(1)]
        ov = o_ref[...]
        g = g_ref[...]
        r = _head_rms(ov, masks)
        gw = g * w_ref[...]
        gwo = gw * ov
        mean = [jnp.sum(jnp.where(mk, gwo, 0.0), axis=1, keepdims=True) * (1.0 / FHD) for mk in masks]
        do = r * gw - ov * (r * r * r) * jnp.where(masks[0], mean[0], mean[1])
        do_ref[...] = do.astype(BF16)
        doo = do * ov
        dl_ref[...] = _pair_cols(*[jnp.sum(jnp.where(mk, doo, 0.0), axis=1, keepdims=True) for mk in masks])

        @pl.when((hp == 0) & (qi == 0))
        def _():
            dw_ref[...] = jnp.zeros_like(dw_ref)

        dw_ref[...] += jnp.sum(g * ov * r, axis=0, keepdims=True)

        @pl.when((hp == NPAIR - 1) & (qi == NQ - 1))
        def _():
            dw = dw_ref[...]
            dw_ref[...] = dw + pltpu.roll(dw, FHD, 1)
            if exchange is not None:
                exchange.finish(*exchange.split(ex_refs))

    blk = pl.BlockSpec((BQ, LANES), lambda hp, i: (i, hp))
    vec = pl.BlockSpec((1, LANES), lambda hp, i: (0, 0))
    res = pl.pallas_call(
        body, name="fox_norm_bwd", grid=(NPAIR, NQ), in_specs=[blk, blk, vec] + ex_in_specs,
        out_specs=[blk, pl.BlockSpec((None, BQ, 2), lambda hp, i: (hp, i, 0)), vec] + ex_out_specs,
        out_shape=[jax.ShapeDtypeStruct((S, DFOX), BF16), jax.ShapeDtypeStruct((NPAIR, S, 2), F32),
                   jax.ShapeDtypeStruct((1, LANES), F32)] + ex_out_shape,
        scratch_shapes=ex_scratch, compiler_params=_cparams(),
    )(o, dmix, w2, *ex_in)
    return res[0], res[1], res[2], res[3:]


def _fox_bwd(proj, do, gates, lse, delta, exchange=None):
    ex_in, ex_in_specs, ex_out_specs, ex_out_shape, ex_scratch = _hosted(exchange)

    pg = FOX_PAIRS_BWD
    n_in = 3 * pg + 4
    heads = [(pp, e) for pp in range(pg) for e in range(2)]

    def body(*refs):
        qkv_refs = refs[:3 * pg]
        do_ref, g_ref, lse_ref, dl_ref = refs[3 * pg:n_in]
        dproj_ref, dc_ref = refs[n_in + len(ex_in):n_in + 2 + len(ex_in)]
        qa_ref, dq_ref = refs[n_in + 2 + len(ex_in) + len(ex_out_shape):n_in + 4 + len(ex_in) + len(ex_out_shape)]
        ex_refs = refs[n_in:n_in + len(ex_in)] + refs[n_in + 2 + len(ex_in):n_in + 2 + len(ex_in) + len(ex_out_shape)] + refs[-2:]
        grp, kj = pl.program_id(0), pl.program_id(1)

        def head_index(pp, e):
            return 2 * (pg * grp + pp) + e

        if exchange is not None:
            @pl.when((grp == 0) & (kj == 0))
            def _():
                exchange.start(*exchange.split(ex_refs))

        @pl.when(kj == 0)
        def _():
            gt = g_ref[...]
            for pp in range(pg):
                qv = qkv_refs[3 * pp][...]
                for e in range(2):
                    qa_ref[2 * pp + e] = _fox_operand(qv, e, _lane_col(gt, head_index(pp, e)), True)
            dq_ref[...] = jnp.zeros_like(dq_ref)

        @pl.when((grp == 0) & (kj == 0))
        def _():
            dc_ref[...] = jnp.zeros_like(dc_ref)

        masks = [_head_mask(0), _head_mask(1)]
        krows = pl.ds(pl.multiple_of(kj * BQ, BQ), BQ)
        gk = g_ref[krows, :]
        kas = [_fox_operand(qkv_refs[3 * pp + 1][...], e, _lane_col(gk, head_index(pp, e)), False) for pp, e in heads]
        vbs = [qkv_refs[3 * pp + 2][...].astype(BF16) for pp in range(pg)]
        lane = lax.broadcasted_iota(jnp.int32, (BQ, LANES), 1)
        n = range(len(heads))

        def block(qi, carry, diagonal):
            dks, dvs, css = carry
            rows = pl.ds(pl.multiple_of(qi * BQ, BQ), BQ)
            qa = [qa_ref[i, rows, :] for i in n]
            s = [_dot(qa[i], kas[i], 1, 1) for i in n]
            if diagonal:
                s = [jnp.where(_causal_block(), s[i], -jnp.inf) for i in n]
            dov = [do_ref[rows, pp * LANES:(pp + 1) * LANES] for pp in range(pg)]
            doe = [jnp.where(masks[e], dov[pp], jnp.zeros_like(dov[pp])) for pp, e in heads]
            lse2 = [lse_ref[pp, rows, :] for pp in range(pg)]
            dl2 = [dl_ref[pp, rows, :] for pp in range(pg)]
            p = [jnp.exp(s[i] - _lane_col(lse2[heads[i][0]], heads[i][1])) for i in n]
            dp = [_dot(doe[i], vbs[heads[i][0]], 1, 1) for i in n]
            ds = [p[i] * (dp[i] - _lane_col(dl2[heads[i][0]], heads[i][1])) for i in n]
            dv_part = [_dot(p[i], doe[i], 0, 0) for i in n]
            dk_part = [_dot(ds[i], jnp.where(masks[heads[i][1]], qa[i], jnp.zeros_like(qa[i])), 0, 0) for i in n]
            dq_part = [jnp.where(masks[heads[i][1]], _dot(ds[i], kas[i]), 0.0) for i in n]
            css = tuple(css[i] + jnp.sum(ds[i], axis=0, keepdims=True) for i in n)
            dc = jnp.zeros((BQ, LANES), F32)
            for i in n:
                dc = dc + jnp.where(lane == head_index(*heads[i]), jnp.sum(ds[i], axis=1, keepdims=True), 0.0)
            for pp in range(pg):
                dq_ref[pp, rows, :] += (dq_part[2 * pp] + dq_part[2 * pp + 1]) * FOX_SCALE
            dc_ref[rows, :] += dc
            dks = tuple(dks[pp] + dk_part[2 * pp] + dk_part[2 * pp + 1] for pp in range(pg))
            dvs = tuple(dvs[pp] + dv_part[2 * pp] + dv_part[2 * pp + 1] for pp in range(pg))
            return dks, dvs, css

        zero = jnp.zeros((BQ, LANES), F32)
        first = block(kj, ((zero,) * pg, (zero,) * pg, (jnp.zeros((1, BQ), F32),) * len(heads)), True)
        dks, dvs, css = lax.fori_loop(kj + 1, NQ, lambda qi, carry: block(qi, carry, False), first)
        r = lax.broadcasted_iota(jnp.int32, (BQ, BQ), 0)
        c = lax.broadcasted_iota(jnp.int32, (BQ, BQ), 1)
        dcol = jnp.zeros((BQ, LANES), F32)
        for i in n:
            col = jnp.sum(jnp.where(r == c, css[i], 0.0), axis=1, keepdims=True)
            dcol = dcol + jnp.where(lane == head_index(*heads[i]), col, 0.0)
        dc_ref[krows, :] -= dcol
        for pp in range(pg):
            base = 3 * pp * LANES
            dproj_ref[krows, base + LANES:base + 2 * LANES] = dks[pp].astype(BF16)
            dproj_ref[krows, base + 2 * LANES:base + 3 * LANES] = dvs[pp].astype(BF16)

        @pl.when(kj == NQ - 1)
        def _():
            for pp in range(pg):
                dproj_ref[:, 3 * pp * LANES:(3 * pp + 1) * LANES] = dq_ref[pp].astype(BF16)

        if exchange is not None:
            @pl.when((grp == NPAIR // pg - 1) & (kj == NQ - 1))
            def _():
                exchange.finish(*exchange.split(ex_refs))

    qkv_specs = []
    for pp in range(pg):
        qkv_specs.append(pl.BlockSpec((S, LANES), lambda g, j, pp=pp: (0, 3 * (pg * g + pp))))
        qkv_specs.append(pl.BlockSpec((BQ, LANES), lambda g, j, pp=pp: (j, 3 * (pg * g + pp) + 1)))
        qkv_specs.append(pl.BlockSpec((BQ, LANES), lambda g, j, pp=pp: (j, 3 * (pg * g + pp) + 2)))
    pair = pl.BlockSpec((pg, S, 2), lambda g, j: (g, 0, 0))
    res = pl.pallas_call(
        body, name="fox_bwd", grid=(NPAIR // pg, NQ),
        in_specs=qkv_specs + [pl.BlockSpec((S, pg * LANES), lambda g, j: (0, g)), pl.BlockSpec((S, LANES), lambda g, j: (0, 0)),
                              pair, pair] + ex_in_specs,
        out_specs=[pl.BlockSpec((S, 3 * pg * LANES), lambda g, j: (0, g)), pl.BlockSpec((S, LANES), lambda g, j: (0, 0))]
        + ex_out_specs,
        out_shape=[jax.ShapeDtypeStruct((S, DPROJ_PAD), BF16), jax.ShapeDtypeStruct((S, LANES), F32)] + ex_out_shape,
        scratch_shapes=[pltpu.VMEM((2 * pg, S, LANES), BF16), pltpu.VMEM((pg, S, LANES), F32)] + ex_scratch,
        compiler_params=_cparams(),
    )(*([proj] * (3 * pg)), do, gates, lse, delta, *ex_in)
    return res[0], res[1], res[2:]


NQKV = 3 * NGH
GDN_QSCALE = GHD ** -0.5


def _shift_down(x, s):
    if s == 0:
        return x
    row = lax.broadcasted_iota(jnp.int32, x.shape, 0)
    return jnp.where(row >= s, pltpu.roll(x, s, 0), 0.0)


def _shift_up(x, s):
    if s == 0:
        return x
    n = x.shape[0]
    row = lax.broadcasted_iota(jnp.int32, x.shape, 0)
    return jnp.where(row < n - s, pltpu.roll(x, n - s, 0), 0.0)


def _conv_pre(xv, wv):
    pre = xv * wv[CONV_K - 1:CONV_K, :]
    for j in range(CONV_K - 1):
        pre = pre + _shift_down(xv, CONV_K - 1 - j) * wv[j:j + 1, :]
    return pre


def _l2_factors(b):
    return b < 2 * NGH, jnp.where(b < NGH, GDN_QSCALE, 1.0)


def _gdn_pre(proj, conv_w):
    def body(x_ref, w_ref, o_ref):
        b = pl.program_id(0)
        c = _silu(_conv_pre(x_ref[...], w_ref[...]))
        normed, scale = _l2_factors(b)
        rs = lax.rsqrt(jnp.sum(c * c, axis=-1, keepdims=True) + EPS)
        o_ref[...] = c * jnp.where(normed, rs, 1.0) * scale

    return pl.pallas_call(
        body, name="gdn_pre", grid=(NQKV,),
        in_specs=[pl.BlockSpec((S, GHD), lambda b: (0, BLK_GDN + b)), pl.BlockSpec((CONV_K, GHD), lambda b: (0, b))],
        out_specs=pl.BlockSpec((S, GHD), lambda b: (0, b)),
        out_shape=jax.ShapeDtypeStruct((S, NQKV * GHD), F32), compiler_params=_cparams(),
    )(proj, conv_w)


def _gdn_pre_bwd(proj, conv_w, dqkv, dproj):
    def body(x_ref, w_ref, dy_ref, dproj_in, dx_ref, dw_ref):
        del dproj_in
        b = pl.program_id(0)
        xv = x_ref[...]
        wv = w_ref[...]
        pre = _conv_pre(xv, wv)
        sig = _sigmoid(pre)
        c = pre * sig
        normed, scale = _l2_factors(b)
        g = dy_ref[...] * scale
        rs = lax.rsqrt(jnp.sum(c * c, axis=-1, keepdims=True) + EPS)
        dc_n = rs * g - c * (rs * rs * rs) * jnp.sum(g * c, axis=-1, keepdims=True)
        dc = jnp.where(normed, dc_n, g)
        dpre = dc * sig * (1.0 + pre * (1.0 - sig))
        dx = dpre * wv[CONV_K - 1:CONV_K, :]
        for j in range(CONV_K - 1):
            dx = dx + _shift_up(dpre, CONV_K - 1 - j) * wv[j:j + 1, :]
        dx_ref[...] = dx.astype(BF16)
        for j in range(CONV_K):
            dw_ref[j:j + 1, :] = jnp.sum(dpre * _shift_down(xv, CONV_K - 1 - j), axis=0, keepdims=True)

    return pl.pallas_call(
        body, name="gdn_pre_bwd", grid=(NQKV,),
        in_specs=[pl.BlockSpec((S, GHD), lambda b: (0, BLK_GDN + b)), pl.BlockSpec((CONV_K, GHD), lambda b: (0, b)),
                  pl.BlockSpec((None, S, GHD), lambda b: (b // NGH, 0, b % NGH)), pl.BlockSpec(memory_space=pl.ANY)],
        out_specs=[pl.BlockSpec((S, GHD), lambda b: (0, BLK_GDN + b)), pl.BlockSpec((CONV_K, GHD), lambda b: (0, b))],
        out_shape=[jax.ShapeDtypeStruct((S, DPROJ_PAD), BF16), jax.ShapeDtypeStruct((CONV_K, NQKV * GHD), F32)],
        input_output_aliases={3: 0}, compiler_params=_cparams(),
    )(proj, conv_w, dqkv, dproj)


CB = 16
NCB = NCH // CB


def _chunk_prep(qs, ks, vs, gcols, bcols, t_saved=None):
    n = range(len(qs))
    r = lax.broadcasted_iota(jnp.int32, (CHUNK, CHUNK), 0)
    c = lax.broadcasted_iota(jnp.int32, (CHUNK, CHUNK), 1)
    incl = c <= r
    eye = (r == c).astype(F32)
    grow = [jnp.sum(gcols[i] * eye, axis=0, keepdims=True) for i in n]
    gc_col = [jnp.sum(jnp.where(incl, grow[i], 0.0), axis=1, keepdims=True) for i in n]
    gc_row = [jnp.sum(jnp.where(r <= c, gcols[i], 0.0), axis=0, keepdims=True) for i in n]
    decay = [jnp.exp(jnp.where(incl, gc_col[i] - gc_row[i], -jnp.inf)) for i in n]
    kb = [ks[i] * bcols[i] for i in n]
    vb = [vs[i] * bcols[i] for i in n]
    kk = [_mm_nt(kb[i], ks[i]) for i in n]
    m = [jnp.where(c < r, kk[i] * decay[i], 0.0) for i in n]
    if t_saved is None:
        t_inv = [eye - m[i] for i in n]
        p = [_dot3(m[i], m[i]) for i in n]
        for step in range(5):
            t_inv = [t_inv[i] + _dot3(t_inv[i], p[i]) for i in n]
            if step < 4:
                p = [_dot3(p[i], p[i]) for i in n]
    else:
        t_inv = [_saved_inverse(m[i], t_saved[i]) for i in n]
    egc = [jnp.exp(gc_col[i]) for i in n]
    u = [_mm_nn(t_inv[i], vb[i]) for i in n]
    w = [_mm_nn(t_inv[i], kb[i] * egc[i]) for i in n]
    qk = [_mm_nt(qs[i], ks[i]) for i in n]
    gc_last = [gc_col[i][CHUNK - 1:CHUNK, :] for i in n]
    return [(u[i], w[i], qk[i] * decay[i], qs[i] * egc[i], ks[i] * jnp.exp(gc_last[i] - gc_col[i]), jnp.exp(gc_last[i]),
             t_inv[i]) for i in n]


def _prep_specs():
    rows = CB * CHUNK
    qs = pl.BlockSpec((rows, GHD), lambda i, h: (i, h))
    ks = pl.BlockSpec((rows, GHD), lambda i, h: (i, NGH + h))
    vs = pl.BlockSpec((rows, GHD), lambda i, h: (i, 2 * NGH + h))
    gs = pl.BlockSpec((rows, LANES), lambda i, h: (i, 0))
    a_s = pl.BlockSpec((None, rows, CHUNK), lambda i, h: (h, i, 0))
    gl_s = pl.BlockSpec((None, CB, 1, LANES), lambda i, h: (h, i, 0, 0))
    return qs, ks, vs, gs, a_s, gl_s


def _gdn_prep(qkv, gates, exchange=None):
    ex_in, ex_in_specs, ex_out_specs, ex_out_shape, ex_scratch = _hosted(exchange)

    def body(*refs):
        q_ref, k_ref, v_ref, g_ref = refs[:4]
        u_ref, w_ref, qd_ref, kd_ref, a_ref, gl_ref, t_ref = refs[4 + len(ex_in):11 + len(ex_in)]
        ex_refs = refs[4:4 + len(ex_in)] + refs[11 + len(ex_in):]
        h = pl.program_id(1)

        if exchange is not None:
            @pl.when((pl.program_id(0) == 0) & (h == 0))
            def _():
                exchange.start(*exchange.split(ex_refs))

        chunks = [pl.ds(cidx * CHUNK, CHUNK) for cidx in range(CB)]
        gts = [g_ref[rows, :] for rows in chunks]
        outs = _chunk_prep([q_ref[rows, :] for rows in chunks], [k_ref[rows, :] for rows in chunks],
                           [v_ref[rows, :] for rows in chunks], [_lane_col(gt, LANE_G + h) for gt in gts],
                           [_lane_col(gt, LANE_BETA + h) for gt in gts])
        for cidx, rows in enumerate(chunks):
            u, w, a, qd, kd, gl, t_inv = outs[cidx]
            u_ref[rows, :] = u
            w_ref[rows, :] = w
            qd_ref[rows, :] = qd
            kd_ref[rows, :] = kd
            a_ref[rows, :] = a
            t_ref[rows, :] = t_inv
            gl_ref[cidx] = jnp.broadcast_to(gl, (1, LANES))

        if exchange is not None:
            @pl.when((pl.program_id(0) == NCB - 1) & (h == NGH - 1))
            def _():
                exchange.finish(*exchange.split(ex_refs))

    qs, ks, vs, gs, a_s, gl_s = _prep_specs()
    tok = jax.ShapeDtypeStruct((S, DGDN), F32)
    sq = jax.ShapeDtypeStruct((NGH, S, CHUNK), F32)
    res = pl.pallas_call(
        body, name="gdn_prep", grid=(NCB, NGH), in_specs=[qs, ks, vs, gs] + ex_in_specs,
        out_specs=[qs, qs, qs, qs, a_s, gl_s, a_s] + ex_out_specs,
        out_shape=[tok, tok, tok, tok, sq, jax.ShapeDtypeStruct((NGH, NCH, 1, LANES), F32), sq] + ex_out_shape,
        scratch_shapes=ex_scratch, compiler_params=_cparams(),
    )(qkv, qkv, qkv, gates, *ex_in)
    return res[:7], res[7:]


def _gdn_prep_bwd(qkv, gates, t_inv, du, dw, dqd, dkd, da, dgl, exchange=None):
    ex_in, ex_in_specs, ex_out_specs, ex_out_shape, ex_scratch = _hosted(exchange)

    def body(*refs):
        q_ref, k_ref, v_ref, g_ref, t_ref, du_ref, dw_ref, dqd_ref, dkd_ref, da_ref, dgl_ref = refs[:11]
        dqkv_ref, dg_ref = refs[11 + len(ex_in):13 + len(ex_in)]
        ex_refs = refs[11:11 + len(ex_in)] + refs[13 + len(ex_in):]
        h = pl.program_id(1)

        if exchange is not None:
            @pl.when((pl.program_id(0) == 0) & (h == 0))
            def _():
                exchange.start(*exchange.split(ex_refs))

        @pl.when(h == 0)
        def _():
            dg_ref[...] = jnp.zeros_like(dg_ref)

        lane = lax.broadcasted_iota(jnp.int32, (CHUNK, LANES), 1)
        chunks = [pl.ds(cidx * CHUNK, CHUNK) for cidx in range(CB)]
        gts = [g_ref[rows, :] for rows in chunks]
        t_saved = [t_ref[rows, :] for rows in chunks]
        _, vjp = jax.vjp(lambda *args: [o[:6] for o in _chunk_prep(*args, t_saved=t_saved)],
                         [q_ref[rows, :] for rows in chunks], [k_ref[rows, :] for rows in chunks],
                         [v_ref[rows, :] for rows in chunks], [_lane_col(gt, LANE_G + h) for gt in gts],
                         [_lane_col(gt, LANE_BETA + h) for gt in gts])
        dqs, dks, dvs, dgcs, dbcs = vjp([(du_ref[rows, :], dw_ref[rows, :], da_ref[rows, :], dqd_ref[rows, :],
                                          dkd_ref[rows, :], dgl_ref[cidx][:, 0:1]) for cidx, rows in enumerate(chunks)])
        for cidx, rows in enumerate(chunks):
            dq, dk, dv, dgc, dbc = dqs[cidx], dks[cidx], dvs[cidx], dgcs[cidx], dbcs[cidx]
            dqkv_ref[0, rows, :] = dq
            dqkv_ref[1, rows, :] = dk
            dqkv_ref[2, rows, :] = dv
            dg_ref[rows, :] += jnp.where(lane == LANE_G + h, dgc, 0.0) + jnp.where(lane == LANE_BETA + h, dbc, 0.0)

        if exchange is not None:
            @pl.when((pl.program_id(0) == NCB - 1) & (h == NGH - 1))
            def _():
                exchange.finish(*exchange.split(ex_refs))

    qs, ks, vs, gs, a_s, gl_s = _prep_specs()
    res = pl.pallas_call(
        body, name="gdn_prep_bwd", grid=(NCB, NGH), in_specs=[qs, ks, vs, gs, a_s, qs, qs, qs, qs, a_s, gl_s] + ex_in_specs,
        out_specs=[pl.BlockSpec((3, CB * CHUNK, GHD), lambda i, h: (0, i, h)), gs] + ex_out_specs,
        out_shape=[jax.ShapeDtypeStruct((3, S, DGDN), F32), jax.ShapeDtypeStruct((S, LANES), F32)] + ex_out_shape,
        scratch_shapes=ex_scratch, compiler_params=_cparams(),
    )(qkv, qkv, qkv, gates, t_inv, du, dw, dqd, dkd, da, dgl, *ex_in)
    return res[0], res[1], res[2:]


def _scan_specs(nh, parts, reverse):
    wide, rows, chunks = nh * GHD, S // parts, NCH // parts

    def part(p):
        return parts - 1 - p if reverse else p

    hs = pl.BlockSpec((rows, wide), lambda g, p: (part(p), g))
    a_s = pl.BlockSpec((nh, rows, CHUNK), lambda g, p: (g, part(p), 0))
    gl_s = pl.BlockSpec((nh, chunks, 1, LANES), lambda g, p: (g, part(p), 0, 0))
    st_s = pl.BlockSpec((nh, chunks, GHD, GHD), lambda g, p: (g, part(p), 0, 0))
    gz_s = pl.BlockSpec((rows, wide), lambda g, p: (part(p), BLK_GZ // nh + g))
    mix_s = pl.BlockSpec((rows, wide), lambda g, p: (part(p), NPAIR // nh + g))
    return hs, a_s, gl_s, st_s, gz_s, mix_s


def _head_cols(hh):
    return slice(hh * GHD, (hh + 1) * GHD)


SCAN_HEADS, SCAN_PARTS = 4, 2
SCAN_HEADS_BWD, SCAN_PARTS_BWD = 2, 2


def _gdn_scan(u, w, qd, kd, a, gl, proj, w_norm, mix):
    heads = range(SCAN_HEADS)

    def body(u_ref, w_ref, qd_ref, kd_ref, a_ref, gl_ref, z_ref, wn_ref, mix_in, mix_ref, o_ref, st_ref, carry_ref):
        del mix_in

        @pl.when(pl.program_id(1) == 0)
        def _():
            carry_ref[...] = jnp.zeros_like(carry_ref)

        def step(ci, states):
            rows = pl.ds(pl.multiple_of(ci * CHUNK, CHUNK), CHUNK)
            for hh in heads:
                st_ref[hh, ci] = states[hh]
            ws = [_dot(w_ref[rows, _head_cols(hh)], states[hh]) for hh in heads]
            qs = [_dot(qd_ref[rows, _head_cols(hh)], states[hh]) for hh in heads]
            vn = [u_ref[rows, _head_cols(hh)] - ws[hh] for hh in heads]
            av = [_dot(a_ref[hh, rows, :], vn[hh]) for hh in heads]
            kv = [_dot(kd_ref[rows, _head_cols(hh)], vn[hh], 0, 0) for hh in heads]
            for hh in heads:
                o_ref[rows, _head_cols(hh)] = qs[hh] + av[hh]
            return tuple(states[hh] * gl_ref[hh, ci] + kv[hh] for hh in heads)

        last = lax.fori_loop(0, NCH // SCAN_PARTS, step, tuple(carry_ref[hh] for hh in heads))
        for hh in heads:
            carry_ref[hh] = last[hh]
            ov = o_ref[:, _head_cols(hh)]
            mix_ref[:, _head_cols(hh)] = (ov * _rms_scale(ov) * wn_ref[...] * _silu(z_ref[:, _head_cols(hh)])).astype(BF16)

    hs, a_s, gl_s, st_s, gz_s, mix_s = _scan_specs(SCAN_HEADS, SCAN_PARTS, False)
    return pl.pallas_call(
        body, name="gdn_scan", grid=(NGH // SCAN_HEADS, SCAN_PARTS),
        in_specs=[hs, hs, hs, hs, a_s, gl_s, gz_s, pl.BlockSpec((1, GHD), lambda g, p: (0, 0)),
                  pl.BlockSpec(memory_space=pl.ANY)],
        out_specs=[mix_s, hs, st_s],
        out_shape=[jax.ShapeDtypeStruct((S, D), BF16), jax.ShapeDtypeStruct((S, DGDN), F32),
                   jax.ShapeDtypeStruct((NGH, NCH, GHD, GHD), F32)],
        input_output_aliases={8: 0}, scratch_shapes=[pltpu.VMEM((SCAN_HEADS, GHD, GHD), F32)], compiler_params=_cparams(),
    )(u, w, qd, kd, a, gl, proj, w_norm, mix)


def _gdn_scan_bwd(dmix, o, proj, w_norm, u, w, qd, kd, a, gl, states, dproj, exchange=None):
    ex_in, ex_in_specs, ex_out_specs, ex_out_shape, ex_scratch = _hosted(exchange)
    groups = NGH // SCAN_HEADS_BWD

    def body(*refs):
        dy_ref, o_ref, z_ref, wn_ref, u_ref, w_ref, qd_ref, kd_ref, a_ref, gl_ref, st_ref = refs[:11]
        dz_ref, du_ref, dw_ref, dqd_ref, dkd_ref, da_ref, dgl_ref, dwn_ref = refs[12 + len(ex_in):20 + len(ex_in)]
        do_ref, carry_ref = refs[20 + len(ex_in) + len(ex_out_shape):22 + len(ex_in) + len(ex_out_shape)]
        ex_refs = refs[12:12 + len(ex_in)] + refs[20 + len(ex_in):20 + len(ex_in) + len(ex_out_shape)] + refs[-2:]
        heads = range(SCAN_HEADS_BWD)
        chunks = NCH // SCAN_PARTS_BWD

        if exchange is not None:
            @pl.when((pl.program_id(0) == 0) & (pl.program_id(1) == 0))
            def _():
                exchange.start(*exchange.split(ex_refs))

        @pl.when((pl.program_id(0) == 0) & (pl.program_id(1) == 0))
        def _():
            dwn_ref[...] = jnp.zeros_like(dwn_ref)

        @pl.when(pl.program_id(1) == 0)
        def _():
            carry_ref[...] = jnp.zeros_like(carry_ref)

        wn = wn_ref[...]
        for hh in heads:
            c = _head_cols(hh)
            ov = o_ref[:, c]
            zv = z_ref[:, c]
            g = dy_ref[:, c]
            sig = _sigmoid(zv)
            dz_ref[:, c] = (g * (ov * _rms_scale(ov) * wn) * sig * (1.0 + zv * (1.0 - sig))).astype(BF16)
            do, dwt = _rms_bwd(ov, wn, g * zv * sig)
            do_ref[:, c] = do
            dwn_ref[...] += jnp.sum(dwt, axis=0, keepdims=True)

        def step(t, dstates):
            ci = chunks - 1 - t
            rows = pl.ds(pl.multiple_of(ci * CHUNK, CHUNK), CHUNK)
            cols = [_head_cols(hh) for hh in heads]
            state = [st_ref[hh, ci] for hh in heads]
            dov = [do_ref[rows, cols[hh]] for hh in heads]
            wv = [w_ref[rows, cols[hh]] for hh in heads]
            ws = [_dot(wv[hh], state[hh]) for hh in heads]
            adov = [_dot(a_ref[hh, rows, :], dov[hh], 0, 0) for hh in heads]
            kds = [_dot(kd_ref[rows, cols[hh]], dstates[hh]) for hh in heads]
            dqd = [_dot(dov[hh], state[hh], 1, 1) for hh in heads]
            qdo = [_dot(qd_ref[rows, cols[hh]], dov[hh], 0, 0) for hh in heads]
            vn = [u_ref[rows, cols[hh]] - ws[hh] for hh in heads]
            dvn = [adov[hh] + kds[hh] for hh in heads]
            da = [_dot(dov[hh], vn[hh], 1, 1) for hh in heads]
            dkd = [_dot(vn[hh], dstates[hh], 1, 1) for hh in heads]
            dwv = [_dot(dvn[hh], state[hh], 1, 1) for hh in heads]
            wdv = [_dot(wv[hh], dvn[hh], 0, 0) for hh in heads]
            for hh in heads:
                da_ref[hh, rows, :] = da[hh]
                dqd_ref[rows, cols[hh]] = dqd[hh]
                dkd_ref[rows, cols[hh]] = dkd[hh]
                dgl = jnp.sum(jnp.sum(dstates[hh] * state[hh], axis=1, keepdims=True), axis=0, keepdims=True)
                dgl_ref[hh, ci] = jnp.broadcast_to(dgl, (1, LANES))
                du_ref[rows, cols[hh]] = dvn[hh]
                dw_ref[rows, cols[hh]] = -dwv[hh]
            return tuple(dstates[hh] * gl_ref[hh, ci] + qdo[hh] - wdv[hh] for hh in heads)

        last = lax.fori_loop(0, chunks, step, tuple(carry_ref[hh] for hh in heads))
        for hh in heads:
            carry_ref[hh] = last[hh]

        if exchange is not None:
            @pl.when((pl.program_id(0) == groups - 1) & (pl.program_id(1) == SCAN_PARTS_BWD - 1))
            def _():
                exchange.finish(*exchange.split(ex_refs))

    hs, a_s, gl_s, st_s, gz_s, mix_s = _scan_specs(SCAN_HEADS_BWD, SCAN_PARTS_BWD, True)
    vec = pl.BlockSpec((1, GHD), lambda g, p: (0, 0))
    tok = jax.ShapeDtypeStruct((S, DGDN), F32)
    res = pl.pallas_call(
        body, name="gdn_scan_bwd", grid=(groups, SCAN_PARTS_BWD),
        in_specs=[mix_s, hs, gz_s, vec, hs, hs, hs, hs, a_s, gl_s, st_s, pl.BlockSpec(memory_space=pl.ANY)] + ex_in_specs,
        out_specs=[gz_s, hs, hs, hs, hs, a_s, gl_s, vec] + ex_out_specs,
        out_shape=[jax.ShapeDtypeStruct((S, DPROJ_PAD), BF16), tok, tok, tok, tok,
                   jax.ShapeDtypeStruct((NGH, S, CHUNK), F32), jax.ShapeDtypeStruct((NGH, NCH, 1, LANES), F32),
                   jax.ShapeDtypeStruct((1, GHD), F32)] + ex_out_shape,
        input_output_aliases={11: 0},
        scratch_shapes=[pltpu.VMEM((S // SCAN_PARTS_BWD, SCAN_HEADS_BWD * GHD), F32),
                        pltpu.VMEM((SCAN_HEADS_BWD, GHD, GHD), F32)] + ex_scratch,
        compiler_params=_cparams(),
    )(dmix, o, proj, w_norm, u, w, qd, kd, a, gl, states, dproj, *ex_in)
    return res[:8], res[8:]


def _place():
    return lax.axis_index("x"), lax.axis_index("y"), lax.axis_index("c")


def _other_chips(x, y):
    return [(1 - x, y), (x, 1 - y), (1 - x, 1 - y)]


HBM = pl.BlockSpec(memory_space=pltpu.HBM)
VMEM = pl.BlockSpec(memory_space=pltpu.VMEM)


def _half_rows(ref_or_rows, half):
    rows = ref_or_rows // 2
    return pl.ds(pl.multiple_of(half * rows, rows), rows)


class _Exchange:
    def __init__(self, inputs, out_shape, n_sems, start, finish):
        self.inputs, self.out_shape, self.n_sems, self.start, self.finish = inputs, out_shape, n_sems, start, finish

    def sem_shapes(self):
        return [pltpu.SemaphoreType.DMA((self.n_sems,)), pltpu.SemaphoreType.DMA((self.n_sems,))]

    def split(self, refs):
        n_in, n_out = len(self.inputs), len(self.out_shape)
        return refs[:n_in], refs[n_in:n_in + n_out], refs[n_in + n_out], refs[n_in + n_out + 1]


def _run_exchange(ex, name):
    def body(*refs):
        parts = ex.split(refs)
        ex.start(*parts)
        ex.finish(*parts)

    return pl.pallas_call(
        body, name=name, in_specs=[HBM] * len(ex.inputs), out_specs=[HBM] * len(ex.out_shape), out_shape=ex.out_shape,
        scratch_shapes=ex.sem_shapes(), compiler_params=_cparams(),
    )(*ex.inputs)


def _allgather_exchange(shards, whole=()):
    n, nw = len(shards), len(whole)

    def plan(src, outs, send_sems, recv_sems):
        x, y, c = _place()
        chips = _other_chips(x, y)
        chip_ids = [2 * ch[0] + ch[1] for ch in chips]

        def copy(a, k, chip_index, half, to, from_src):
            rows = _half_rows(src[a].shape[0], half)
            dst = outs[a].at[chip_index, rows]
            return pltpu.make_async_remote_copy(
                src_ref=src[a].at[rows] if from_src else dst, dst_ref=dst, send_sem=send_sems.at[6 * a + k],
                recv_sem=recv_sems.at[6 * a + k], device_id=to, device_id_type=MESH)

        def whole_copy(b, k, chip_index, to):
            return pltpu.make_async_remote_copy(
                src_ref=src[n + b], dst_ref=outs[n + b].at[chip_index], send_sem=send_sems.at[6 * n + 3 * b + k],
                recv_sem=recv_sems.at[6 * n + 3 * b + k], device_id=to, device_id_type=MESH)

        me, sibling = (x, y, c), (x, y, 1 - c)
        first = [copy(a, j, 2 * x + y, c, (*chips[j], c), True) for a in range(n) for j in range(3)]
        first += [whole_copy(b, j, 2 * x + y, (*chips[j], c)) for b in range(nw) for j in range(3)]
        landing = [copy(a, j, chip_ids[j], c, me, False) for a in range(n) for j in range(3)]
        passed = [copy(a, 3 + j, chip_ids[j], c, sibling, False) for a in range(n) for j in range(3)]
        arriving = [copy(a, 3 + j, chip_ids[j], 1 - c, me, False) for a in range(n) for j in range(3)]
        arriving += [whole_copy(b, j, chip_ids[j], me) for b in range(nw) for j in range(3)]
        return first, landing, passed, arriving

    def start(*refs):
        for cp in plan(*refs)[0]:
            cp.start()

    def finish(*refs):
        first, landing, passed, arriving = plan(*refs)
        for lands, onward in zip(landing, passed):
            lands.wait_recv()
            onward.start()
        for cp in arriving:
            cp.wait_recv()
        for cp in first + passed:
            cp.wait_send()

    out_shape = [jax.ShapeDtypeStruct((NCHIP,) + s.shape, s.dtype) for s in list(shards) + list(whole)]
    return _Exchange(list(shards) + list(whole), out_shape, 6 * n + 3 * nw, start, finish)


def _with_own(gathered, own):
    x, y, _ = _place()
    return lax.dynamic_update_index_in_dim(gathered, own, 2 * x + y, axis=0)


def _simple_exchange(inputs, out_shape, copies_of):
    def start(*refs):
        for cp in copies_of(*refs):
            cp.start()

    def finish(*refs):
        for cp in copies_of(*refs):
            cp.wait()

    return _Exchange(list(inputs), out_shape, len(out_shape) * 3, start, finish)


def _pair_exchange(grads):
    def copies_of(src, outs, send_sems, recv_sems):
        x, y, c = _place()
        return [pltpu.make_async_remote_copy(
            src_ref=src[a].at[:, _half_rows(src[a].shape[1], 1 - c)], dst_ref=outs[a], send_sem=send_sems.at[a],
            recv_sem=recv_sems.at[a], device_id=(x, y, 1 - c), device_id_type=MESH) for a in range(len(src))]

    return _simple_exchange(
        grads, [jax.ShapeDtypeStruct((g.shape[0], g.shape[1] // 2, g.shape[2]), g.dtype) for g in grads], copies_of)


def _pair_sum(grads, theirs, name):
    n = len(grads)

    def body(*refs):
        south = lax.axis_index("c") == 0
        for a in range(n):
            g = refs[a][...]
            half = g.shape[0] // 2
            mine = jnp.where(south, g[:half], g[half:])
            refs[2 * n + a][...] = (mine.astype(F32) + refs[n + a][...].astype(F32)).astype(BF16)

    def specs(arrs):
        return [pl.BlockSpec((None,) + g.shape[1:], lambda j: (j, 0, 0)) for g in arrs]

    return pl.pallas_call(
        body, name=name, grid=(NCHIP,), in_specs=specs(grads) + specs(theirs), out_specs=specs(theirs),
        out_shape=[jax.ShapeDtypeStruct(g.shape, BF16) for g in theirs], compiler_params=_cparams(),
    )(*grads, *theirs)


def _chip_exchange(parts):
    def copies_of(src, outs, send_sems, recv_sems):
        x, y, c = _place()
        return [pltpu.make_async_remote_copy(
            src_ref=src[a].at[2 * chip[0] + chip[1]], dst_ref=outs[a].at[k], send_sem=send_sems.at[3 * a + k],
            recv_sem=recv_sems.at[3 * a + k], device_id=(*chip, c), device_id_type=MESH)
            for a in range(len(src)) for k, chip in enumerate(_other_chips(x, y))]

    return _simple_exchange(parts, [jax.ShapeDtypeStruct((NCHIP - 1,) + p.shape[1:], p.dtype) for p in parts], copies_of)


def _chip_sum(parts, received, exchange=None):
    n = len(parts)
    steps = 4
    ex_in, ex_in_specs, ex_out_specs, ex_out_shape, ex_scratch = _hosted(exchange)

    def body(*refs):
        ex_refs = refs[2 * n:2 * n + len(ex_in)] + refs[3 * n + len(ex_in):]
        if exchange is not None:
            @pl.when(pl.program_id(0) == 0)
            def _():
                exchange.start(*exchange.split(ex_refs))

        chip = 2 * lax.axis_index("x") + lax.axis_index("y")
        for a in range(n):
            p, r = refs[a], refs[n + a]
            own = jnp.where(chip == 0, p[0], jnp.where(chip == 1, p[1], jnp.where(chip == 2, p[2], p[3])))
            refs[2 * n + len(ex_in) + a][...] = ((own.astype(F32) + r[0].astype(F32)) + r[1].astype(F32)) + r[2].astype(F32)

        if exchange is not None:
            @pl.when(pl.program_id(0) == steps - 1)
            def _():
                exchange.finish(*exchange.split(ex_refs))

    def specs(arrs):
        return [pl.BlockSpec((g.shape[0], g.shape[1] // steps, g.shape[2]), lambda i: (0, i, 0)) for g in arrs]

    out_specs = [pl.BlockSpec((g.shape[1] // steps, g.shape[2]), lambda i: (i, 0)) for g in parts]
    res = pl.pallas_call(
        body, name="grads_chip_sum", grid=(steps,), in_specs=specs(parts) + specs(received) + ex_in_specs,
        out_specs=out_specs + ex_out_specs, out_shape=[jax.ShapeDtypeStruct(g.shape[1:], F32) for g in parts] + ex_out_shape,
        scratch_shapes=ex_scratch, compiler_params=_cparams(),
    )(*parts, *received, *ex_in)
    return res[:n], res[n:]


def _pair_share(halves):
    def copies_of(src, outs, send_sems, recv_sems):
        x, y, c = _place()
        return [pltpu.make_async_remote_copy(
            src_ref=src[a], dst_ref=outs[a], send_sem=send_sems.at[a], recv_sem=recv_sems.at[a],
            device_id=(x, y, 1 - c), device_id_type=MESH) for a in range(len(src))]

    return _simple_exchange(halves, [jax.ShapeDtypeStruct(h.shape, F32) for h in halves], copies_of)


def _adamw_math(w, g, m, v):
    nm = ADAM_B1 * m + (1.0 - ADAM_B1) * g
    nv = ADAM_B2 * v + (1.0 - ADAM_B2) * jnp.square(g)
    m_hat = nm / (1.0 - ADAM_B1 ** ADAM_STEP)
    v_hat = nv / (1.0 - ADAM_B2 ** ADAM_STEP)
    return -ADAM_LR * (m_hat / (jnp.sqrt(v_hat) + ADAM_EPS) + ADAM_WD * w), nm, nv


def _adamw_big(ws, g_mine, g_theirs, ms, vs, exchange=None):
    n = len(ws)
    steps = 8
    ex_in, ex_in_specs, ex_out_specs, ex_out_shape, ex_scratch = _hosted(exchange)

    def body(*refs):
        ex_refs = refs[5 * n:5 * n + len(ex_in)] + refs[9 * n + len(ex_in):]
        outs = refs[5 * n + len(ex_in):9 * n + len(ex_in)]
        if exchange is not None:
            @pl.when(pl.program_id(0) == 0)
            def _():
                exchange.start(*exchange.split(ex_refs))

        own_half = (pl.program_id(0) // (steps // 2)) == lax.axis_index("c")
        for a in range(n):
            g = jnp.where(own_half, refs[n + a][...], refs[2 * n + a][...])
            d, nm, nv = _adamw_math(refs[a][...], g, refs[3 * n + a][...], refs[4 * n + a][...])
            outs[a][...] = g
            outs[n + a][...] = d
            outs[2 * n + a][...] = nm
            outs[3 * n + a][...] = nv

        if exchange is not None:
            @pl.when(pl.program_id(0) == steps - 1)
            def _():
                exchange.finish(*exchange.split(ex_refs))

    specs = [pl.BlockSpec((w.shape[0] // steps, w.shape[1]), lambda i: (i, 0)) for w in ws]
    half_specs = [pl.BlockSpec((g.shape[0] // (steps // 2), g.shape[1]), lambda i: (i % (steps // 2), 0)) for g in g_mine]
    shapes = [jax.ShapeDtypeStruct(w.shape, F32) for w in ws]
    res = pl.pallas_call(
        body, name="adamw_big", grid=(steps,), in_specs=specs + half_specs * 2 + specs * 2 + ex_in_specs,
        out_specs=specs * 4 + ex_out_specs, out_shape=shapes * 4 + ex_out_shape, scratch_shapes=ex_scratch,
        compiler_params=_cparams(),
    )(*ws, *g_mine, *g_theirs, *ms, *vs, *ex_in)
    return res[:n], res[n:2 * n], res[2 * n:3 * n], res[3 * n:4 * n], res[4 * n:]


def _adamw_in(w, g_mine, g_theirs, m, v):
    half = D // 2

    def body(w_ref, gm_ref, gt_ref, m_ref, v_ref, g_out, d_out, nm_out, nv_out, g_ref):
        south = lax.axis_index("c") == 0
        g_ref[0:half, :] = jnp.where(south, gm_ref[...], gt_ref[...])
        g_ref[half:D, :] = jnp.where(south, gt_ref[...], gm_ref[...])
        g = g_ref[0:CW, :]
        d, nm, nv = _adamw_math(w_ref[...], g, m_ref[...], v_ref[...])
        g_out[...] = g
        d_out[...] = d
        nm_out[...] = nm
        nv_out[...] = nv

    spec = pl.BlockSpec((CW, LANES), lambda i: (0, i))
    half_spec = pl.BlockSpec((half, LANES), lambda i: (0, i))
    return pl.pallas_call(
        body, name="adamw_in", grid=(D // LANES,), in_specs=[spec, half_spec, half_spec, spec, spec], out_specs=[spec] * 4,
        out_shape=[jax.ShapeDtypeStruct((CW, D), F32)] * 4, scratch_shapes=[pltpu.VMEM((D, LANES), F32)],
        compiler_params=_cparams(),
    )(w, g_mine, g_theirs, m, v)


NORM_NAMES = ("pre_mix_norm", "post_mix_norm", "pre_mlp_norm", "post_mlp_norm")
SMALL_NAMES = NORM_NAMES + ("gdn_conv_w", "fox_f_bias", "gdn_dt_bias", "gdn_a_log", "fox_out_norm", "gdn_out_norm")
CONV_COLS = 3 * DGDN // NCHIP


def _small_gather(d_norms, d_conv, sums, d_fox_norm, d_gdn_norm, loss_row):
    n_arrays = 6
    n_remote = n_arrays * (NDEV - 1)

    def copies_of(src, outs, send_sems, recv_sems):
        x, y, c = _place()
        me = 4 * x + 2 * y + c

        def from_me(chip_index):
            cols = pl.ds(pl.multiple_of(chip_index * CONV_COLS, LANES), CONV_COLS)
            return [src[0], src[1].at[:, cols], src[2], src[3], src[4], src[5]]

        local = [pltpu.make_async_copy(s, outs[a].at[me], send_sems.at[n_remote + a]) for a, s in enumerate(from_me(2 * x + y))]
        remote = []
        for k in range(1, NDEV):
            px, py, pc = x ^ ((k >> 2) & 1), y ^ ((k >> 1) & 1), c ^ (k & 1)
            remote += [pltpu.make_async_remote_copy(
                src_ref=s, dst_ref=outs[a].at[me], send_sem=send_sems.at[n_arrays * (k - 1) + a],
                recv_sem=recv_sems.at[n_arrays * (k - 1) + a], device_id=(px, py, pc), device_id_type=MESH)
                for a, s in enumerate(from_me(2 * px + py))]
        return local + remote

    def start(*refs):
        for cp in copies_of(*refs):
            cp.start()

    def finish(*refs):
        for cp in copies_of(*refs):
            cp.wait()

    shapes = [(4, D), (CONV_K, CONV_COLS), (8, LANES), (1, LANES), (1, LANES), (1, LANES)]
    return _Exchange([d_norms, d_conv, sums, d_fox_norm, d_gdn_norm, loss_row],
                     [jax.ShapeDtypeStruct((NDEV,) + s, F32) for s in shapes], n_remote + n_arrays, start, finish)


def _small_adamw(gathered, ws, ms, vs):
    n = len(SMALL_NAMES)
    ng = len(gathered)

    def body(*refs):
        def total(buf):
            acc = buf[0]
            for i in range(1, NDEV):
                acc = acc + buf[i]
            return acc

        t_norms, t_conv, t_sums, t_fn, t_gn, t_loss = [total(r) for r in refs[:ng]]
        w_refs, m_refs, v_refs = refs[ng:ng + n], refs[ng + n:ng + 2 * n], refs[ng + 2 * n:ng + 3 * n]
        outs = refs[ng + 3 * n:]
        outs[4 * n][...] = t_loss
        grads = [t_norms[i:i + 1, :] for i in range(4)] + [
            t_conv, t_sums[0:1, 0:NFH], t_sums[1:2, 0:NGH], t_sums[2:3, 0:NGH], t_fn[:, 0:FHD], t_gn]
        for a in range(n):
            d, nm, nv = _adamw_math(w_refs[a][...], grads[a], m_refs[a][...], v_refs[a][...])
            outs[a][...] = grads[a]
            outs[n + a][...] = d
            outs[2 * n + a][...] = nm
            outs[3 * n + a][...] = nv

    def whole(arr):
        return pl.BlockSpec(arr.shape, lambda i: (0,) * arr.ndim)

    res = pl.pallas_call(
        body, name="small_adamw", grid=(1,), in_specs=[whole(t) for t in gathered] + [whole(w) for w in ws] * 3,
        out_specs=[whole(w) for w in ws] * 4 + [pl.BlockSpec((1, LANES), lambda i: (0, 0))],
        out_shape=[jax.ShapeDtypeStruct(w.shape, F32) for w in ws] * 4 + [jax.ShapeDtypeStruct((1, LANES), F32)],
        compiler_params=_cparams(),
    )(*gathered, *ws, *ms, *vs)
    return res[:n], res[n:2 * n], res[2 * n:3 * n], res[3 * n:4 * n], res[4 * n]


CW = DPROJ // NCHIP
PROJ_RUNS = tuple((part * DFOX + hp * LANES, part * DFOX + (hp + 1) * LANES, (3 * hp + part) * LANES)
                  for hp in range(NPAIR) for part in range(3)) + (
    (1536, 1544, BLK_SMALL * LANES), (1544, 3080, BLK_GDN * LANES), (3080, 3088, BLK_SMALL * LANES + 8),
    (3088, 3600, BLK_GZ * LANES))


def _proj_pieces():
    pieces = []
    for lo, hi, at in PROJ_RUNS:
        while lo < hi:
            j = lo // CW
            end = min(hi, (j + 1) * CW)
            pieces.append((j, lo - j * CW, at, end - lo))
            at, lo = at + end - lo, end
    return pieces


RT = 256


def _to_padded_rows(gathered):
    def body(src_ref, out_ref, blocks_ref, rows_ref):
        blocks_ref[...] = src_ref[...].astype(F32)
        rows_ref[...] = jnp.zeros_like(rows_ref)
        for j, start, at, n in _proj_pieces():
            rows_ref[at:at + n, :] = blocks_ref[j, start:start + n, :]
        out_ref[...] = rows_ref[...].astype(out_ref.dtype)

    return pl.pallas_call(
        body, name="proj_rows_in", grid=(D // RT,), in_specs=[pl.BlockSpec((NCHIP, D, RT), lambda i: (0, 0, i))],
        out_specs=pl.BlockSpec((DPROJ_PAD, RT), lambda i: (0, i)), out_shape=jax.ShapeDtypeStruct((DPROJ_PAD, D), gathered.dtype),
        scratch_shapes=[pltpu.VMEM((NCHIP, D, RT), F32), pltpu.VMEM((DPROJ_PAD, RT), F32)], compiler_params=_cparams(),
    )(gathered)


def _from_padded_rows(w):
    def body(src_ref, out_ref, rows_ref, blocks_ref):
        rows_ref[...] = src_ref[...].astype(F32)
        blocks_ref[...] = jnp.zeros_like(blocks_ref)
        for j, start, at, n in _proj_pieces():
            blocks_ref[j, start:start + n, :] = rows_ref[at:at + n, :]
        out_ref[...] = blocks_ref[...].astype(out_ref.dtype)

    return pl.pallas_call(
        body, name="proj_rows_out", grid=(D // RT,), in_specs=[pl.BlockSpec((DPROJ_PAD, RT), lambda i: (0, i))],
        out_specs=pl.BlockSpec((NCHIP, D, RT), lambda i: (0, 0, i)), out_shape=jax.ShapeDtypeStruct((NCHIP, D, D), w.dtype),
        scratch_shapes=[pltpu.VMEM((DPROJ_PAD, RT), F32), pltpu.VMEM((NCHIP, D, RT), F32)], compiler_params=_cparams(),
    )(w)


def _local_step(x, target, first_weights, late_weights, reduce_late, reduce_in, pre_mix_norm, fox_f_bias, fox_out_norm,
                gdn_a_log, gdn_dt_bias, gdn_out_norm, post_mix_norm, pre_mlp_norm, post_mlp_norm):
    bias_vec = jnp.zeros((1, LANES), F32).at[0, 0:NFH].set(fox_f_bias).at[0, LANE_G:LANE_G + NGH].set(gdn_dt_bias)
    alog_vec = jnp.zeros((1, LANES), F32).at[0, LANE_G:LANE_G + NGH].set(gdn_a_log)
    w2 = jnp.concatenate([fox_out_norm, fox_out_norm], axis=1)

    h, first = _pre_norm(x, pre_mix_norm, exchange=first_weights[0])
    win_p, conv_w = first_weights[1](first)
    proj = _matmul(h, win_p, tb=True, tm=2048, tn=768, tk=1024, name="mm_proj", exchange=late_weights[0])
    proj, late_a = proj if late_weights[0] is not None else (proj, [])
    gates = _gates(proj, bias_vec, alog_vec)
    mix, fox_o, lse, late_b = _fox_fwd(proj, gates, w2, exchange=late_weights[1])
    qkv = _gdn_pre(proj, conv_w)
    (u, w, qd, kd, a_intra, gl, t_inv), late_c = _gdn_prep(qkv, gates, exchange=late_weights[2])
    wout, wup3, wdown = late_weights[3](late_a, late_b, late_c)
    mix, gdn_raw, states = _gdn_scan(u, w, qd, kd, a_intra, gl, proj, gdn_out_norm, mix)
    mixed = _matmul(mix, wout, tm=2048, tk=1024, name="mm_out")
    x1, h2 = _post_mix(x, mixed, post_mix_norm, pre_mlp_norm)

    def relu2(acc):
        r = jnp.maximum(acc, 0.0)
        return r, r * r

    up_relu, act = _matmul(h2, wup3, b3=True, tm=1024, tn=1024, tk=1024, out_dtypes=(BF16, BF16), epilogue=relu2,
                           name="mm_up")
    y = _matmul(act, wdown, tm=1024, tk=DFF, name="mm_down")
    dx2, dy, d_post_mlp, loss_row = _loss_head(x1, y, post_mlp_norm, target)

    dwdown = _matmul(act, dy, ta=True, tm=1024, tn=1024, tk=2048, out_dtypes=(BF16,), name="mm_dwdown")

    def relu2_bwd(acc, r):
        return (acc * 2.0 * r.astype(F32),)

    dup = _matmul(dy, wdown, tb=True, tm=1024, tn=1024, tk=1024, out_dtypes=(BF16,), extra=(up_relu,), epilogue=relu2_bwd,
                  name="mm_dact")
    dwup3 = _matmul(h2, dup, ta=True, tm=1024, tn=1024, tk=2048, out_dtypes=(BF16,), o3=True, name="mm_dwup")
    dh2 = _matmul(dup, wup3, tb=True, b3=True, tm=1024, tk=DFF, name="mm_dh2")
    dx1, dmixed, d_pre_mlp, d_post_mix = _mid_bwd(dh2, x1, pre_mlp_norm, dx2, mixed, post_mix_norm)
    dwout = _matmul(mix, dmixed, ta=True, tm=1024, tn=1024, tk=2048, out_dtypes=(BF16,), name="mm_dwout")
    dmix = _matmul(dmixed, wout, tb=True, tm=2048, tk=1024, name="mm_dmix")

    dfox, delta, d_fox_norm, from_sibling = _fox_norm_bwd(fox_o, dmix, w2, exchange=reduce_late[0](dwout, dwup3, dwdown))
    dproj, dcum_fox, reduced_a = _fox_bwd(proj, dfox, gates, lse, delta, exchange=reduce_late[1](from_sibling))
    (dproj, du, dw, dqd, dkd, da, dgl, d_gdn_norm), reduced_b = _gdn_scan_bwd(
        dmix, gdn_raw, proj, gdn_out_norm, u, w, qd, kd, a_intra, gl, states, dproj, exchange=reduce_late[2]())
    dqkv, dgates_gdn, reduced_c = _gdn_prep_bwd(qkv, gates, t_inv, du, dw, dqd, dkd, da, dgl, exchange=reduce_late[3]())
    reduced_late = (reduced_a, reduced_b, reduced_c)
    dproj, d_conv = _gdn_pre_bwd(proj, conv_w, dqkv, dproj)
    dproj, sums = _gates_bwd(proj, bias_vec, alog_vec, dgates_gdn, dcum_fox, dproj)

    dwin_p = _matmul(dproj, h, ta=True, tm=1280, tn=1024, tk=2048, out_dtypes=(BF16,), name="mm_dwin")
    exchange_in = reduce_in(dwin_p)
    dh = _matmul(dproj, win_p, tm=1024, tk=DPROJ_PAD, name="mm_dh", exchange=exchange_in)
    dh, reduced_in = dh if exchange_in is not None else (dh, [])
    grad_x, d_pre_mix = _pre_norm_bwd(dh, x, pre_mix_norm, dx1)

    d_norms = jnp.concatenate([d_pre_mix, d_post_mix, d_pre_mlp, d_post_mlp], axis=0)
    return grad_x, (d_norms, d_conv, sums, d_fox_norm, d_gdn_norm, loss_row), reduced_late, reduced_in


def kernel(x, pre_mix_norm, w_in, fox_f_bias, fox_out_norm, gdn_conv_w, gdn_a_log, gdn_dt_bias, gdn_out_norm, w_out, post_mix_norm, pre_mlp_norm, w_up, w_down, post_mlp_norm, loss_target, m_pre_mix_norm, m_w_in, m_fox_f_bias, m_fox_out_norm, m_gdn_conv_w, m_gdn_a_log, m_gdn_dt_bias, m_gdn_out_norm, m_w_out, m_post_mix_norm, m_pre_mlp_norm, m_w_up, m_w_down, m_post_mlp_norm, v_pre_mix_norm, v_w_in, v_fox_f_bias, v_fox_out_norm, v_gdn_conv_w, v_gdn_a_log, v_gdn_dt_bias, v_gdn_out_norm, v_w_out, v_post_mix_norm, v_pre_mlp_norm, v_w_up, v_w_down, v_post_mlp_norm):
    weights = dict(pre_mix_norm=pre_mix_norm, w_in=w_in, fox_f_bias=fox_f_bias, fox_out_norm=fox_out_norm, gdn_conv_w=gdn_conv_w,
                   gdn_a_log=gdn_a_log, gdn_dt_bias=gdn_dt_bias, gdn_out_norm=gdn_out_norm, w_out=w_out, post_mix_norm=post_mix_norm,
                   pre_mlp_norm=pre_mlp_norm, w_up=w_up, w_down=w_down, post_mlp_norm=post_mlp_norm)
    m_in = dict(pre_mix_norm=m_pre_mix_norm, w_in=m_w_in, fox_f_bias=m_fox_f_bias, fox_out_norm=m_fox_out_norm, gdn_conv_w=m_gdn_conv_w,
                gdn_a_log=m_gdn_a_log, gdn_dt_bias=m_gdn_dt_bias, gdn_out_norm=m_gdn_out_norm, w_out=m_w_out, post_mix_norm=m_post_mix_norm,
                pre_mlp_norm=m_pre_mlp_norm, w_up=m_w_up, w_down=m_w_down, post_mlp_norm=m_post_mlp_norm)
    v_in = dict(pre_mix_norm=v_pre_mix_norm, w_in=v_w_in, fox_f_bias=v_fox_f_bias, fox_out_norm=v_fox_out_norm, gdn_conv_w=v_gdn_conv_w,
                gdn_a_log=v_gdn_a_log, gdn_dt_bias=v_gdn_dt_bias, gdn_out_norm=v_gdn_out_norm, w_out=v_w_out, post_mix_norm=v_post_mix_norm,
                pre_mlp_norm=v_pre_mlp_norm, w_up=v_w_up, w_down=v_w_down, post_mlp_norm=v_post_mlp_norm)
    order_w = ("pre_mix_norm", "w_in", "fox_f_bias", "fox_out_norm", "gdn_conv_w", "gdn_a_log", "gdn_dt_bias", "gdn_out_norm", "w_out",
               "post_mix_norm", "pre_mlp_norm", "w_up", "w_down", "post_mlp_norm")
    big = ("w_in", "w_out", "w_up", "w_down")

    def row(v):
        return v if v.ndim == 2 else v.reshape(1, -1)

    win_shard = jnp.pad(w_in.T.astype(BF16), ((0, D - CW), (0, 0)))

    def resolve_first(gathered):
        win_g, conv_g = gathered
        return (_to_padded_rows(_with_own(win_g, win_shard)),
                _with_own(conv_g, gdn_conv_w).transpose(1, 0, 2).reshape(CONV_K, 3 * DGDN))

    late_shards = [weights[n].astype(BF16) for n in big[1:]]

    def resolve_late(*gathered):
        wout_g, wup3, wdown_g = [_with_own(g[0], own) for g, own in zip(gathered, late_shards)]
        return wout_g.reshape(D, D), wup3, wdown_g.reshape(DFF, D)

    pair_sums, late_blocks = {}, []

    def pair_summed(names, blocks, theirs):
        for n, s in zip(names, _pair_sum(blocks, theirs, "grads_pair_sum_" + names[0])):
            pair_sums[n] = s

    def late_pair_exchange(dwout, dwup3, dwdown):
        late_blocks.extend([dwout.reshape(NCHIP, D // NCHIP, D), dwup3, dwdown.reshape(NCHIP, DFF // NCHIP, D)])
        return _pair_exchange(late_blocks)

    def late_chip_exchange(theirs):
        pair_summed(big[1:], late_blocks, theirs)
        return _chip_exchange([pair_sums["w_up"]])

    def reduce_in(dwin_p):
        blocks = [_from_padded_rows(dwin_p)]
        pair_summed(big[:1], blocks, _run_exchange(_pair_exchange(blocks), "grads_pair_exchange_w_in"))
        return _chip_exchange([pair_sums["w_in"]])

    grad_x, small, received_late, received_in = _local_step(
        x[0], loss_target[0], (_allgather_exchange([win_shard], whole=[gdn_conv_w]), resolve_first),
        tuple(_allgather_exchange([shard]) for shard in late_shards) + (resolve_late,),
        (late_pair_exchange, late_chip_exchange, lambda: _chip_exchange([pair_sums["w_down"]]),
         lambda: _chip_exchange([pair_sums["w_out"]])),
        reduce_in, row(pre_mix_norm), fox_f_bias, row(fox_out_norm), gdn_a_log, gdn_dt_bias,
        row(gdn_out_norm), row(post_mix_norm), row(pre_mlp_norm), row(post_mlp_norm))
    received_up, received_down, received_out = received_late

    g_mine, small_gathered = _chip_sum(
        [pair_sums[n] for n in big], list(received_in) + list(received_out) + list(received_up) + list(received_down),
        exchange=_small_gather(*small))
    g_theirs = _run_exchange(_pair_share(g_mine), "grads_pair_share")

    g_big, d_big, nm_big, nv_big, _ = _adamw_big(
        [weights[n] for n in big[1:]], g_mine[1:], g_theirs[1:], [m_in[n] for n in big[1:]], [v_in[n] for n in big[1:]])
    in_t = _adamw_in(w_in.T, g_mine[0], g_theirs[0], m_w_in.T, v_w_in.T)
    g_small, d_small, nm_small, nv_small, loss_total = _small_adamw(
        small_gathered, [row(weights[n]) for n in SMALL_NAMES], [row(m_in[n]) for n in SMALL_NAMES],
        [row(v_in[n]) for n in SMALL_NAMES])

    grads, delta, new_m, new_v = {}, {}, {}, {}
    grads["w_in"], delta["w_in"], new_m["w_in"], new_v["w_in"] = [t.T for t in in_t]
    for i, n in enumerate(big[1:]):
        grads[n], delta[n], new_m[n], new_v[n] = g_big[i], d_big[i], nm_big[i], nv_big[i]
    for i, n in enumerate(SMALL_NAMES):
        shape = weights[n].shape
        grads[n], delta[n], new_m[n], new_v[n] = (g_small[i].reshape(shape), d_small[i].reshape(shape),
                                                  nm_small[i].reshape(shape), nv_small[i].reshape(shape))
    return (loss_total[0, 0], grad_x[None], *[grads[n] for n in order_w], *[delta[n] for n in order_w], *[new_m[n] for n in order_w],
            *[new_v[n] for n in order_w])
```

```python
import jax
import jax.numpy as jnp
from jax import lax
from jax.experimental import pallas as pl
from jax.experimental.pallas import tpu as pltpu

F32 = jnp.float32
BF16 = jnp.bfloat16
MESH = pl.DeviceIdType.MESH

S = 2048
D = 1024
NFH, FHD = 8, 64
NPAIR = NFH // 2
NGH, GHD = 4, 128
DFOX = NFH * FHD
DGDN = NGH * GHD
CHUNK = 64
NCH = S // CHUNK
CONV_K = 4
DFF = 4 * D
EPS = 1e-6
DPROJ = 3600
LANES = 128
DPROJ_PAD = 3840
BLK_GDN = 12
BLK_GZ = 24
BLK_SMALL = 28
NCHIP = 4
NDEV = 8
VMEM_LIMIT = 56 * 1024 * 1024

ADAM_LR = 0.001
ADAM_B1 = 0.9
ADAM_B2 = 0.999
ADAM_EPS = 1e-08
ADAM_WD = 0.01
ADAM_STEP = 10


def _cparams(**kw):
    return pltpu.CompilerParams(vmem_limit_bytes=VMEM_LIMIT, **kw)


def _dn(ca, cb):
    return (((ca,), (cb,)), ((), ()))


def _dot(a, b, ca=1, cb=0):
    return lax.dot_general(a.astype(BF16), b.astype(BF16), _dn(ca, cb), preferred_element_type=F32)


def _hdot(a, b, ca=1, cb=0):
    return lax.dot_general(a.astype(F32), b.astype(F32), _dn(ca, cb), precision=lax.Precision.HIGHEST,
                           preferred_element_type=F32)


def _dot3(a, b, ca=1, cb=0):
    a_hi, b_hi = a.astype(BF16), b.astype(BF16)
    a_lo, b_lo = (a - a_hi.astype(F32)).astype(BF16), (b - b_hi.astype(F32)).astype(BF16)
    dn = _dn(ca, cb)
    return (lax.dot_general(a_hi, b_hi, dn, preferred_element_type=F32)
            + (lax.dot_general(a_hi, b_lo, dn, preferred_element_type=F32)
               + lax.dot_general(a_lo, b_hi, dn, preferred_element_type=F32)))


@jax.custom_vjp
def _mm_nn(a, b):
    return _dot(a, b, 1, 0)


def _mm_nn_fwd(a, b):
    return _dot(a, b, 1, 0), (a, b)


def _mm_nn_bwd(res, g):
    a, b = res
    return _dot(g, b, 1, 1), _dot(a, g, 0, 0)


_mm_nn.defvjp(_mm_nn_fwd, _mm_nn_bwd)


@jax.custom_vjp
def _mm_nt(a, b):
    return _dot(a, b, 1, 1)


def _mm_nt_fwd(a, b):
    return _dot(a, b, 1, 1), (a, b)


def _mm_nt_bwd(res, g):
    a, b = res
    return _dot(g, b, 1, 0), _dot(g, a, 0, 0)


_mm_nt.defvjp(_mm_nt_fwd, _mm_nt_bwd)


@jax.custom_vjp
def _saved_inverse(m, t_inv):
    del m
    return t_inv


def _saved_inverse_fwd(m, t_inv):
    del m
    return t_inv, t_inv


def _saved_inverse_bwd(t_inv, g):
    return -_dot3(_dot3(t_inv, g, 0, 0), t_inv, 1, 1), jnp.zeros_like(t_inv)


_saved_inverse.defvjp(_saved_inverse_fwd, _saved_inverse_bwd)


def _sigmoid(z):
    return 1.0 / (1.0 + jnp.exp(-z))


def _softplus(z):
    return jnp.maximum(z, 0.0) + jnp.log(1.0 + jnp.exp(-jnp.abs(z)))


def _silu(z):
    return z * _sigmoid(z)


def _rms_scale(x):
    return lax.rsqrt(jnp.mean(x * x, axis=-1, keepdims=True) + EPS)


def _rms_bwd(x, w, g):
    r = _rms_scale(x)
    gw = g * w
    dx = r * gw - x * (r * r * r) * jnp.mean(gw * x, axis=-1, keepdims=True)
    return dx, g * x * r


def _matmul(a, b, *, name, ta=False, tb=False, tm=512, tn=512, tk=512, out_dtypes=(F32,), b3=False, o3=False,
            extra=(), epilogue=None, exchange=None):
    m, k = (a.shape[1], a.shape[0]) if ta else a.shape
    if b3:
        n = b.shape[1] if tb else b.shape[0] * b.shape[2]
        kb = b.shape[0] * b.shape[2] if tb else b.shape[1]
    else:
        n, kb = (b.shape[0], b.shape[1]) if tb else (b.shape[1], b.shape[0])
    assert kb == k, (name, kb, k)
    tm, tn, tk = min(tm, m), min(tn, n), min(tk, k)
    assert m % tm == 0 and n % tn == 0 and k % tk == 0, (name, m, n, k, tm, tn, tk)
    nk = k // tk
    whole_k_blocks = b3 and tb and not ta and nk == 1 and b.shape[0] > 1
    n_extra = len(extra)
    n_out = len(out_dtypes)
    grid = (m // tm, n // tn, nk)
    ex_in, ex_in_specs, ex_out_specs, ex_out_shape, ex_scratch = _hosted(exchange)

    def body(*refs):
        a_ref, b_ref = refs[0], refs[1]
        extra_refs = refs[2:2 + n_extra]
        first_out = 2 + n_extra + len(ex_in)
        out_refs = refs[first_out:first_out + n_out]
        ex_refs = refs[2 + n_extra:first_out] + refs[first_out + n_out:first_out + n_out + len(ex_out_shape)] + refs[-2:]
        step = [pl.program_id(d) for d in range(3)]

        if exchange is not None:
            @pl.when((step[0] == 0) & (step[1] == 0) & (step[2] == 0))
            def _():
                exchange.start(*exchange.split(ex_refs))

        def finish(acc):
            outs = (acc,) if epilogue is None else epilogue(acc, *[r[...] for r in extra_refs])
            for o_ref, val in zip(out_refs, outs):
                o_ref[...] = val.astype(o_ref.dtype)

        if whole_k_blocks:
            width = b.shape[2]
            part = _dot(a_ref[:, 0:width], b_ref[0], 1, 1)
            for blk in range(1, b.shape[0]):
                part = part + _dot(a_ref[:, blk * width:(blk + 1) * width], b_ref[blk], 1, 1)
        else:
            part = _dot(a_ref[...], b_ref[...], 0 if ta else 1, 1 if tb else 0)
        if nk == 1:
            finish(part)
        else:
            acc_ref = refs[first_out + n_out + len(ex_out_shape)]

            @pl.when(step[2] == 0)
            def _():
                acc_ref[...] = part

            @pl.when(step[2] > 0)
            def _():
                acc_ref[...] += part

            @pl.when(step[2] == nk - 1)
            def _():
                finish(acc_ref[...])

        if exchange is not None:
            @pl.when((step[0] == grid[0] - 1) & (step[1] == grid[1] - 1) & (step[2] == nk - 1))
            def _():
                exchange.finish(*exchange.split(ex_refs))

    a_spec = pl.BlockSpec((tk, tm), lambda i, j, kk: (kk, i)) if ta else pl.BlockSpec((tm, tk), lambda i, j, kk: (i, kk))
    if whole_k_blocks:
        b_spec = pl.BlockSpec((b.shape[0], tn, b.shape[2]), lambda i, j, kk: (0, j, 0))
    elif b3 and tb:
        assert b.shape[2] == tk
        b_spec = pl.BlockSpec((None, tn, tk), lambda i, j, kk: (kk, j, 0))
    elif b3:
        assert b.shape[2] == tn
        b_spec = pl.BlockSpec((None, tk, tn), lambda i, j, kk: (j, kk, 0))
    elif tb:
        b_spec = pl.BlockSpec((tn, tk), lambda i, j, kk: (j, kk))
    else:
        b_spec = pl.BlockSpec((tk, tn), lambda i, j, kk: (kk, j))
    tile = pl.BlockSpec((tm, tn), lambda i, j, kk: (i, j))
    out_specs = [tile] * n_out
    out_shape = [jax.ShapeDtypeStruct((m, n), dt) for dt in out_dtypes]
    if o3:
        out_specs[0] = pl.BlockSpec((None, tm, tn), lambda i, j, kk: (j, i, 0))
        out_shape[0] = jax.ShapeDtypeStruct((n // tn, m, tn), out_dtypes[0])
    res = pl.pallas_call(
        body, name=name, grid=grid,
        in_specs=[a_spec, b_spec] + [tile] * n_extra + ex_in_specs, out_specs=out_specs + ex_out_specs,
        out_shape=out_shape + ex_out_shape,
        scratch_shapes=([pltpu.VMEM((tm, tn), F32)] if nk > 1 else []) + ex_scratch,
        compiler_params=_cparams(),
    )(a, b, *extra, *ex_in)
    if exchange is not None:
        return (res[0] if n_out == 1 else res[:n_out]), res[n_out:]
    return res[0] if n_out == 1 else res


TR = 256


def _row_spec(cols):
    return pl.BlockSpec((TR, cols), lambda i: (i, 0))


def _vec_spec(cols):
    return pl.BlockSpec((1, cols), lambda i: (0, 0))


def _pre_norm(x, w, exchange=None):
    ex_in, ex_in_specs, ex_out_specs, ex_out_shape, ex_scratch = _hosted(exchange)

    def body(*refs):
        x_ref, w_ref, h_ref = refs[0], refs[1], refs[2 + len(ex_in)]
        ex_refs = refs[2:2 + len(ex_in)] + refs[3 + len(ex_in):]
        if exchange is not None:
            @pl.when(pl.program_id(0) == 0)
            def _():
                exchange.start(*exchange.split(ex_refs))

        xv = x_ref[...]
        h_ref[...] = (xv * _rms_scale(xv) * w_ref[...]).astype(BF16)

        if exchange is not None:
            @pl.when(pl.program_id(0) == S // TR - 1)
            def _():
                exchange.finish(*exchange.split(ex_refs))

    res = pl.pallas_call(
        body, name="pre_norm", grid=(S // TR,), in_specs=[_row_spec(D), _vec_spec(D)] + ex_in_specs,
        out_specs=[_row_spec(D)] + ex_out_specs, out_shape=[jax.ShapeDtypeStruct((S, D), BF16)] + ex_out_shape,
        scratch_shapes=ex_scratch, compiler_params=_cparams(),
    )(x, w, *ex_in)
    return res[0], res[1:]


def _post_mix(x, mixed, w_post, w_pre_mlp):
    def body(x_ref, m_ref, wp_ref, wm_ref, x1_ref, h2_ref):
        mv = m_ref[...]
        x1 = x_ref[...] + mv * _rms_scale(mv) * wp_ref[...]
        x1_ref[...] = x1
        h2_ref[...] = (x1 * _rms_scale(x1) * wm_ref[...]).astype(BF16)

    return pl.pallas_call(
        body, name="post_mix", grid=(S // TR,),
        in_specs=[_row_spec(D), _row_spec(D), _vec_spec(D), _vec_spec(D)], out_specs=[_row_spec(D), _row_spec(D)],
        out_shape=[jax.ShapeDtypeStruct((S, D), F32), jax.ShapeDtypeStruct((S, D), BF16)], compiler_params=_cparams(),
    )(x, mixed, w_post, w_pre_mlp)


def _loss_head(x1, y, w_post_mlp, target):
    def body(x1_ref, y_ref, w_ref, t_ref, dx2_ref, dy_ref, dw_ref, loss_ref):
        i = pl.program_id(0)
        yv = y_ref[...]
        w = w_ref[...]
        x2 = x1_ref[...] + yv * _rms_scale(yv) * w
        err = x2 - t_ref[...]
        dx2 = err * (1.0 / D)
        dx2_ref[...] = dx2
        dy, dwt = _rms_bwd(yv, w, dx2)
        dy_ref[...] = dy.astype(BF16)

        @pl.when(i == 0)
        def _():
            dw_ref[...] = jnp.zeros_like(dw_ref)
            loss_ref[...] = jnp.zeros_like(loss_ref)

        dw_ref[...] += jnp.sum(dwt, axis=0, keepdims=True)
        part = 0.5 * jnp.sum(jnp.mean(err * err, axis=-1, keepdims=True), axis=0, keepdims=True)
        loss_ref[...] += jnp.broadcast_to(part, loss_ref.shape)

    return pl.pallas_call(
        body, name="loss_head", grid=(S // TR,),
        in_specs=[_row_spec(D), _row_spec(D), _vec_spec(D), _row_spec(D)],
        out_specs=[_row_spec(D), _row_spec(D), _vec_spec(D), _vec_spec(LANES)],
        out_shape=[jax.ShapeDtypeStruct((S, D), F32), jax.ShapeDtypeStruct((S, D), BF16),
                   jax.ShapeDtypeStruct((1, D), F32), jax.ShapeDtypeStruct((1, LANES), F32)],
        compiler_params=_cparams(),
    )(x1, y, w_post_mlp, target)


def _mid_bwd(dh2, x1, w_pre_mlp, dx2, mixed, w_post):
    def body(dh2_ref, x1_ref, wm_ref, dx2_ref, m_ref, wp_ref, dx1_ref, dm_ref, dwm_ref, dwp_ref):
        i = pl.program_id(0)
        dxa, dwm = _rms_bwd(x1_ref[...], wm_ref[...], dh2_ref[...])
        dx1 = dx2_ref[...] + dxa
        dx1_ref[...] = dx1
        dm, dwp = _rms_bwd(m_ref[...], wp_ref[...], dx1)
        dm_ref[...] = dm.astype(BF16)

        @pl.when(i == 0)
        def _():
            dwm_ref[...] = jnp.zeros_like(dwm_ref)
            dwp_ref[...] = jnp.zeros_like(dwp_ref)

        dwm_ref[...] += jnp.sum(dwm, axis=0, keepdims=True)
        dwp_ref[...] += jnp.sum(dwp, axis=0, keepdims=True)

    return pl.pallas_call(
        body, name="mid_bwd", grid=(S // TR,),
        in_specs=[_row_spec(D), _row_spec(D), _vec_spec(D), _row_spec(D), _row_spec(D), _vec_spec(D)],
        out_specs=[_row_spec(D), _row_spec(D), _vec_spec(D), _vec_spec(D)],
        out_shape=[jax.ShapeDtypeStruct((S, D), F32), jax.ShapeDtypeStruct((S, D), BF16),
                   jax.ShapeDtypeStruct((1, D), F32), jax.ShapeDtypeStruct((1, D), F32)],
        compiler_params=_cparams(),
    )(dh2, x1, w_pre_mlp, dx2, mixed, w_post)


def _pre_norm_bwd(dh, x, w, dx1):
    def body(dh_ref, x_ref, w_ref, dx1_ref, dx_ref, dw_ref):
        i = pl.program_id(0)
        dxa, dwt = _rms_bwd(x_ref[...], w_ref[...], dh_ref[...])
        dx_ref[...] = dx1_ref[...] + dxa

        @pl.when(i == 0)
        def _():
            dw_ref[...] = jnp.zeros_like(dw_ref)

        dw_ref[...] += jnp.sum(dwt, axis=0, keepdims=True)

    return pl.pallas_call(
        body, name="pre_norm_bwd", grid=(S // TR,),
        in_specs=[_row_spec(D), _row_spec(D), _vec_spec(D), _row_spec(D)], out_specs=[_row_spec(D), _vec_spec(D)],
        out_shape=[jax.ShapeDtypeStruct((S, D), F32), jax.ShapeDtypeStruct((1, D), F32)], compiler_params=_cparams(),
    )(dh, x, w, dx1)


BQ = 512
NQ = S // BQ
LANE_BETA, LANE_G = 8, 12


def _gate_lanes(shape):
    lane = lax.broadcasted_iota(jnp.int32, shape, 1)
    return lane < LANE_BETA, (lane >= LANE_BETA) & (lane < LANE_G), (lane >= LANE_G) & (lane < LANE_G + NGH)


def _gates(proj, bias_vec, alog_vec):
    def body(s_ref, b_ref, a_ref, o_ref, carry_ref):
        i = pl.program_id(0)

        @pl.when(i == 0)
        def _():
            carry_ref[...] = jnp.zeros_like(carry_ref)

        z = s_ref[...] + b_ref[...]
        tail = jnp.log(1.0 + jnp.exp(-jnp.abs(z)))
        sp = jnp.maximum(z, 0.0) + tail
        lf = jnp.minimum(z, 0.0) - tail
        r = lax.broadcasted_iota(jnp.int32, (BQ, BQ), 0)
        c = lax.broadcasted_iota(jnp.int32, (BQ, BQ), 1)
        tri = (c <= r).astype(F32)
        cum = _hdot(tri, lf) + carry_ref[...]
        carry_ref[...] = cum[BQ - 1:BQ, :]
        is_fox, is_beta, is_g = _gate_lanes(z.shape)
        o_ref[...] = jnp.where(is_fox, cum, jnp.where(is_beta, _sigmoid(z), jnp.where(is_g, -jnp.exp(a_ref[...]) * sp, 0.0)))

    return pl.pallas_call(
        body, name="gates", grid=(NQ,),
        in_specs=[pl.BlockSpec((BQ, LANES), lambda i: (i, BLK_SMALL)), _vec_spec(LANES), _vec_spec(LANES)],
        out_specs=pl.BlockSpec((BQ, LANES), lambda i: (i, 0)), out_shape=jax.ShapeDtypeStruct((S, LANES), F32),
        scratch_shapes=[pltpu.VMEM((1, LANES), F32)], compiler_params=_cparams(),
    )(proj, bias_vec, alog_vec)


def _gates_bwd(proj, bias_vec, alog_vec, dgates_gdn, dcum_fox, dproj):
    def body(s_ref, b_ref, a_ref, dg_ref, dc_ref, dproj_in, dproj_ref, red_ref, carry_ref):
        del dproj_in
        i = pl.program_id(0)

        @pl.when(i == 0)
        def _():
            carry_ref[...] = jnp.zeros_like(carry_ref)
            red_ref[...] = jnp.zeros_like(red_ref)

        z = s_ref[...] + b_ref[...]
        dg = dg_ref[...] + dc_ref[...]
        r = lax.broadcasted_iota(jnp.int32, (BQ, BQ), 0)
        c = lax.broadcasted_iota(jnp.int32, (BQ, BQ), 1)
        upper = (c >= r).astype(F32)
        dlf = _hdot(upper, dg) + carry_ref[...]
        carry_ref[...] = dlf[0:1, :]
        sig = _sigmoid(z)
        g_scale = -jnp.exp(a_ref[...])
        is_fox, is_beta, is_g = _gate_lanes(z.shape)
        ds = jnp.where(is_fox, dlf * (1.0 - sig), jnp.where(is_beta, dg * sig * (1.0 - sig), jnp.where(is_g, dg * g_scale * sig, 0.0)))
        dproj_ref[:, 0:LANES] = ds.astype(BF16)
        dproj_ref[:, LANES:2 * LANES] = jnp.zeros((BQ, LANES), BF16)
        dalog = jnp.where(is_g, dg * g_scale * _softplus(z), 0.0)
        sums = jnp.sum(ds, axis=0, keepdims=True)
        red_ref[0:1, :] += jnp.where(is_fox[0:1], sums, 0.0)
        red_ref[1:2, :] += pltpu.roll(jnp.where(is_g[0:1], sums, 0.0), LANES - LANE_G, 1)
        red_ref[2:3, :] += pltpu.roll(jnp.sum(dalog, axis=0, keepdims=True), LANES - LANE_G, 1)

    blk = pl.BlockSpec((BQ, LANES), lambda i: (NQ - 1 - i, 0))
    return pl.pallas_call(
        body, name="gates_bwd", grid=(NQ,),
        in_specs=[pl.BlockSpec((BQ, LANES), lambda i: (NQ - 1 - i, BLK_SMALL)), _vec_spec(LANES), _vec_spec(LANES), blk, blk,
                  pl.BlockSpec(memory_space=pl.ANY)],
        out_specs=[pl.BlockSpec((BQ, 2 * LANES), lambda i: (NQ - 1 - i, BLK_SMALL // 2)), pl.BlockSpec((8, LANES), lambda i: (0, 0))],
        out_shape=[jax.ShapeDtypeStruct((S, DPROJ_PAD), BF16), jax.ShapeDtypeStruct((8, LANES), F32)],
        input_output_aliases={5: 0},
        scratch_shapes=[pltpu.VMEM((1, LANES), F32)], compiler_params=_cparams(),
    )(proj, bias_vec, alog_vec, dgates_gdn, dcum_fox, dproj)


FOX_SCALE = FHD ** -0.5
FOX_PAIRS = 2
FOX_PAIRS_BWD = 2


def _head_mask(e):
    lane = lax.broadcasted_iota(jnp.int32, (1, LANES), 1)
    return (lane >= e * FHD) & (lane < (e + 1) * FHD)


def _lane_col(vals, index):
    lane = lax.broadcasted_iota(jnp.int32, vals.shape, 1)
    return jnp.sum(jnp.where(lane == index, vals, 0.0), axis=1, keepdims=True)


def _sublane_row(vals, index):
    row = lax.broadcasted_iota(jnp.int32, vals.shape, 0)
    return jnp.sum(jnp.where(row == index, vals, 0.0), axis=0, keepdims=True)


def _pair_cols(c0, c1):
    lane = lax.broadcasted_iota(jnp.int32, (c0.shape[0], 2), 1)
    return jnp.where(lane == 0, c0, c1)


def _split3(x):
    hi = x.astype(BF16).astype(F32)
    rest = x - hi
    mid = rest.astype(BF16).astype(F32)
    return hi, mid, (rest - mid).astype(BF16).astype(F32)


def _fox_operand(vals, e, cum, is_query):
    lane = lax.broadcasted_iota(jnp.int32, (1, LANES), 1)
    base = (1 - e) * FHD
    parts = _split3(cum)
    own = jnp.where(_head_mask(e), vals * FOX_SCALE if is_query else vals, 0.0)
    cum_at, ones_at = (base, base + 3) if is_query else (base + 3, base)
    sign = 1.0 if is_query else -1.0
    out = own + jnp.where((lane >= ones_at) & (lane < ones_at + 3), 1.0, 0.0)
    for i, part in enumerate(parts):
        out = out + jnp.where(lane == cum_at + i, sign * part, 0.0)
    return out.astype(BF16)


def _causal_block():
    return lax.broadcasted_iota(jnp.int32, (BQ, BQ), 1) <= lax.broadcasted_iota(jnp.int32, (BQ, BQ), 0)


def _head_rms(o, masks):
    o2 = o * o
    r = [lax.rsqrt(jnp.sum(jnp.where(mk, o2, 0.0), axis=1, keepdims=True) * (1.0 / FHD) + EPS) for mk in masks]
    return jnp.where(masks[0], r[0], r[1])


def _hosted(exchange):
    if exchange is None:
        return [], [], [], [], []
    return (exchange.inputs, [HBM] * len(exchange.inputs), [HBM] * len(exchange.out_shape), exchange.out_shape,
            exchange.sem_shapes())


def _fox_fwd(proj, gates, w2, exchange=None):
    ex_in, ex_in_specs, ex_out_specs, ex_out_shape, ex_scratch = _hosted(exchange)

    n_in = 3 * FOX_PAIRS + 2
    heads = [(pp, e) for pp in range(FOX_PAIRS) for e in range(2)]

    def body(*refs):
        qkv_refs, g_ref, w_ref = refs[:3 * FOX_PAIRS], refs[3 * FOX_PAIRS], refs[3 * FOX_PAIRS + 1]
        mix_ref, o_ref, lse_ref = refs[n_in + len(ex_in):n_in + 3 + len(ex_in)]
        ka_ref, vb_ref = refs[n_in + 3 + len(ex_in) + len(ex_out_shape):n_in + 5 + len(ex_in) + len(ex_out_shape)]
        ex_refs = refs[n_in:n_in + len(ex_in)] + refs[n_in + 3 + len(ex_in):n_in + 3 + len(ex_in) + len(ex_out_shape)] + refs[-2:]
        grp, qi = pl.program_id(0), pl.program_id(1)

        def head_index(pp, e):
            return 2 * (FOX_PAIRS * grp + pp) + e

        if exchange is not None:
            @pl.when((grp == 0) & (qi == 0))
            def _():
                exchange.start(*exchange.split(ex_refs))

        @pl.when(qi == 0)
        def _():
            gt = g_ref[...]
            for pp in range(FOX_PAIRS):
                kv = qkv_refs[3 * pp + 1][...]
                for e in range(2):
                    ka_ref[2 * pp + e] = _fox_operand(kv, e, _lane_col(gt, head_index(pp, e)), False)
                vb_ref[pp] = qkv_refs[3 * pp + 2][...].astype(BF16)

        masks = [_head_mask(0), _head_mask(1)]
        gt = g_ref[pl.ds(pl.multiple_of(qi * BQ, BQ), BQ), :]
        qs = [_fox_operand(qkv_refs[3 * pp][...], e, _lane_col(gt, head_index(pp, e)), True) for pp, e in heads]
        n = range(len(heads))

        def block(kj, carry, diagonal):
            rows = pl.ds(pl.multiple_of(kj * BQ, BQ), BQ)
            s = [_dot(qs[i], ka_ref[i, rows, :], 1, 1) for i in n]
            if diagonal:
                s = [jnp.where(_causal_block(), s[i], -jnp.inf) for i in n]
            m_new = [jnp.maximum(carry[i][0], jnp.max(s[i], axis=-1, keepdims=True)) for i in n]
            p = [jnp.exp(s[i] - m_new[i]) for i in n]
            alpha = [jnp.exp(carry[i][0] - m_new[i]) for i in n]
            l_new = [alpha[i] * carry[i][1] + jnp.sum(p[i], axis=-1, keepdims=True) for i in n]
            pv = [_dot(p[i], vb_ref[heads[i][0], rows, :]) for i in n]
            return tuple((m_new[i], l_new[i], alpha[i] * carry[i][2] + pv[i]) for i in n)

        one = (jnp.full((BQ, 1), -jnp.inf, F32), jnp.zeros((BQ, 1), F32), jnp.zeros((BQ, LANES), F32))
        below = lax.fori_loop(0, qi, lambda kj, carry: block(kj, carry, False), (one,) * len(heads))
        done = block(qi, below, True)
        for pp in range(FOX_PAIRS):
            (m0, l0, a0), (m1, l1, a1) = done[2 * pp], done[2 * pp + 1]
            o = jnp.where(masks[0], a0 / l0, a1 / l1)
            cols = slice(pp * LANES, (pp + 1) * LANES)
            o_ref[:, cols] = o
            mix_ref[:, cols] = (o * _head_rms(o, masks) * w_ref[...]).astype(BF16)
            lse_ref[pp] = _pair_cols(m0 + jnp.log(l0), m1 + jnp.log(l1))

        if exchange is not None:
            @pl.when((grp == NPAIR // FOX_PAIRS - 1) & (qi == NQ - 1))
            def _():
                exchange.finish(*exchange.split(ex_refs))

    qkv_specs = []
    for pp in range(FOX_PAIRS):
        qkv_specs.append(pl.BlockSpec((BQ, LANES), lambda g, i, pp=pp: (i, 3 * (FOX_PAIRS * g + pp))))
        qkv_specs.append(pl.BlockSpec((S, LANES), lambda g, i, pp=pp: (0, 3 * (FOX_PAIRS * g + pp) + 1)))
        qkv_specs.append(pl.BlockSpec((S, LANES), lambda g, i, pp=pp: (0, 3 * (FOX_PAIRS * g + pp) + 2)))
    blk = pl.BlockSpec((BQ, FOX_PAIRS * LANES), lambda g, i: (i, g))
    res = pl.pallas_call(
        body, name="fox_fwd", grid=(NPAIR // FOX_PAIRS, NQ),
        in_specs=qkv_specs + [pl.BlockSpec((S, LANES), lambda g, i: (0, 0)), pl.BlockSpec((1, LANES), lambda g, i: (0, 0))]
        + ex_in_specs,
        out_specs=[blk, blk, pl.BlockSpec((FOX_PAIRS, BQ, 2), lambda g, i: (g, i, 0))] + ex_out_specs,
        out_shape=[jax.ShapeDtypeStruct((S, D), BF16), jax.ShapeDtypeStruct((S, DFOX), F32),
                   jax.ShapeDtypeStruct((NPAIR, S, 2), F32)] + ex_out_shape,
        scratch_shapes=[pltpu.VMEM((2 * FOX_PAIRS, S, LANES), BF16), pltpu.VMEM((FOX_PAIRS, S, LANES), BF16)] + ex_scratch,
        compiler_params=_cparams(),
    )(*([proj] * (3 * FOX_PAIRS)), gates, w2, *ex_in)
    return res[0], res[1], res[2], res[3:]


def _fox_norm_bwd(o, dmix, w2, exchange=None):
    ex_in, ex_in_specs, ex_out_specs, ex_out_shape, ex_scratch = _hosted(exchange)

    def body(*refs):
        o_ref, g_ref, w_ref = refs[:3]
        do_ref, dl_ref, dw_ref = refs[3 + len(ex_in):6 + len(ex_in)]
        ex_refs = refs[3:3 + len(ex_in)] + refs[6 + len(ex_in):]
        hp, qi = pl.program_id(0), pl.program_id(1)

        if exchange is not None:
            @pl.when((hp == 0) & (qi == 0))
            def _():
                exchange.start(*exchange.split(ex_refs))

        masks = [_head_mask(0), _head_mask(1)]
        ov = o_ref[...]
        g = g_ref[...]
        r = _head_rms(ov, masks)
        gw = g * w_ref[...]
        gwo = gw * ov
        mean = [jnp.sum(jnp.where(mk, gwo, 0.0), axis=1, keepdims=True) * (1.0 / FHD) for mk in masks]
        do = r * gw - ov * (r * r * r) * jnp.where(masks[0], mean[0], mean[1])
        do_ref[...] = do.astype(BF16)
        doo = do * ov
        dl_ref[...] = _pair_cols(*[jnp.sum(jnp.where(mk, doo, 0.0), axis=1, keepdims=True) for mk in masks])

        @pl.when((hp == 0) & (qi == 0))
        def _():
            dw_ref[...] = jnp.zeros_like(dw_ref)

        dw_ref[...] += jnp.sum(g * ov * r, axis=0, keepdims=True)

        @pl.when((hp == NPAIR - 1) & (qi == NQ - 1))
        def _():
            dw = dw_ref[...]
            dw_ref[...] = dw + pltpu.roll(dw, FHD, 1)
            if exchange is not None:
                exchange.finish(*exchange.split(ex_refs))

    blk = pl.BlockSpec((BQ, LANES), lambda hp, i: (i, hp))
    vec = pl.BlockSpec((1, LANES), lambda hp, i: (0, 0))
    res = pl.pallas_call(
        body, name="fox_norm_bwd", grid=(NPAIR, NQ), in_specs=[blk, blk, vec] + ex_in_specs,
        out_specs=[blk, pl.BlockSpec((None, BQ, 2), lambda hp, i: (hp, i, 0)), vec] + ex_out_specs,
        out_shape=[jax.ShapeDtypeStruct((S, DFOX), BF16), jax.ShapeDtypeStruct((NPAIR, S, 2), F32),
                   jax.ShapeDtypeStruct((1, LANES), F32)] + ex_out_shape,
        scratch_shapes=ex_scratch, compiler_params=_cparams(),
    )(o, dmix, w2, *ex_in)
    return res[0], res[1], res[2], res[3:]


def _fox_bwd(proj, do, gates, lse, delta, exchange=None):
    ex_in, ex_in_specs, ex_out_specs, ex_out_shape, ex_scratch = _hosted(exchange)

    pg = FOX_PAIRS_BWD
    n_in = 3 * pg + 4
    heads = [(pp, e) for pp in range(pg) for e in range(2)]

    def body(*refs):
        qkv_refs = refs[:3 * pg]
        do_ref, g_ref, lse_ref, dl_ref = refs[3 * pg:n_in]
        dproj_ref, dc_ref = refs[n_in + len(ex_in):n_in + 2 + len(ex_in)]
        qa_ref, dq_ref = refs[n_in + 2 + len(ex_in) + len(ex_out_shape):n_in + 4 + len(ex_in) + len(ex_out_shape)]
        ex_refs = refs[n_in:n_in + len(ex_in)] + refs[n_in + 2 + len(ex_in):n_in + 2 + len(ex_in) + len(ex_out_shape)] + refs[-2:]
        grp, kj = pl.program_id(0), pl.program_id(1)

        def head_index(pp, e):
            return 2 * (pg * grp + pp) + e

        if exchange is not None:
            @pl.when((grp == 0) & (kj == 0))
            def _():
                exchange.start(*exchange.split(ex_refs))

        @pl.when(kj == 0)
        def _():
            gt = g_ref[...]
            for pp in range(pg):
                qv = qkv_refs[3 * pp][...]
                for e in range(2):
                    qa_ref[2 * pp + e] = _fox_operand(qv, e, _lane_col(gt, head_index(pp, e)), True)
            dq_ref[...] = jnp.zeros_like(dq_ref)

        @pl.when((grp == 0) & (kj == 0))
        def _():
            dc_ref[...] = jnp.zeros_like(dc_ref)

        masks = [_head_mask(0), _head_mask(1)]
        krows = pl.ds(pl.multiple_of(kj * BQ, BQ), BQ)
        gk = g_ref[krows, :]
        kas = [_fox_operand(qkv_refs[3 * pp + 1][...], e, _lane_col(gk, head_index(pp, e)), False) for pp, e in heads]
        vbs = [qkv_refs[3 * pp + 2][...].astype(BF16) for pp in range(pg)]
        lane = lax.broadcasted_iota(jnp.int32, (BQ, LANES), 1)
        n = range(len(heads))

        def block(qi, carry, diagonal):
            dks, dvs, css = carry
            rows = pl.ds(pl.multiple_of(qi * BQ, BQ), BQ)
            qa = [qa_ref[i, rows, :] for i in n]
            s = [_dot(qa[i], kas[i], 1, 1) for i in n]
            if diagonal:
                s = [jnp.where(_causal_block(), s[i], -jnp.inf) for i in n]
            dov = [do_ref[rows, pp * LANES:(pp + 1) * LANES] for pp in range(pg)]
            doe = [jnp.where(masks[e], dov[pp], jnp.zeros_like(dov[pp])) for pp, e in heads]
            lse2 = [lse_ref[pp, rows, :] for pp in range(pg)]
            dl2 = [dl_ref[pp, rows, :] for pp in range(pg)]
            p = [jnp.exp(s[i] - _lane_col(lse2[heads[i][0]], heads[i][1])) for i in n]
            dp = [_dot(doe[i], vbs[heads[i][0]], 1, 1) for i in n]
            ds = [p[i] * (dp[i] - _lane_col(dl2[heads[i][0]], heads[i][1])) for i in n]
            dv_part = [_dot(p[i], doe[i], 0, 0) for i in n]
            dk_part = [_dot(ds[i], jnp.where(masks[heads[i][1]], qa[i], jnp.zeros_like(qa[i])), 0, 0) for i in n]
            dq_part = [jnp.where(masks[heads[i][1]], _dot(ds[i], kas[i]), 0.0) for i in n]
            css = tuple(css[i] + jnp.sum(ds[i], axis=0, keepdims=True) for i in n)
            dc = jnp.zeros((BQ, LANES), F32)
            for i in n:
                dc = dc + jnp.where(lane == head_index(*heads[i]), jnp.sum(ds[i], axis=1, keepdims=True), 0.0)
            for pp in range(pg):
                dq_ref[pp, rows, :] += (dq_part[2 * pp] + dq_part[2 * pp + 1]) * FOX_SCALE
            dc_ref[rows, :] += dc
            dks = tuple(dks[pp] + dk_part[2 * pp] + dk_part[2 * pp + 1] for pp in range(pg))
            dvs = tuple(dvs[pp] + dv_part[2 * pp] + dv_part[2 * pp + 1] for pp in range(pg))
            return dks, dvs, css

        zero = jnp.zeros((BQ, LANES), F32)
        first = block(kj, ((zero,) * pg, (zero,) * pg, (jnp.zeros((1, BQ), F32),) * len(heads)), True)
        dks, dvs, css = lax.fori_loop(kj + 1, NQ, lambda qi, carry: block(qi, carry, False), first)
        r = lax.broadcasted_iota(jnp.int32, (BQ, BQ), 0)
        c = lax.broadcasted_iota(jnp.int32, (BQ, BQ), 1)
        dcol = jnp.zeros((BQ, LANES), F32)
        for i in n:
            col = jnp.sum(jnp.where(r == c, css[i], 0.0), axis=1, keepdims=True)
            dcol = dcol + jnp.where(lane == head_index(*heads[i]), col, 0.0)
        dc_ref[krows, :] -= dcol
        for pp in range(pg):
            base = 3 * pp * LANES
            dproj_ref[krows, base + LANES:base + 2 * LANES] = dks[pp].astype(BF16)
            dproj_ref[krows, base + 2 * LANES:base + 3 * LANES] = dvs[pp].astype(BF16)

        @pl.when(kj == NQ - 1)
        def _():
            for pp in range(pg):
                dproj_ref[:, 3 * pp * LANES:(3 * pp + 1) * LANES] = dq_ref[pp].astype(BF16)

        if exchange is not None:
            @pl.when((grp == NPAIR // pg - 1) & (kj == NQ - 1))
            def _():
                exchange.finish(*exchange.split(ex_refs))

    qkv_specs = []
    for pp in range(pg):
        qkv_specs.append(pl.BlockSpec((S, LANES), lambda g, j, pp=pp: (0, 3 * (pg * g + pp))))
        qkv_specs.append(pl.BlockSpec((BQ, LANES), lambda g, j, pp=pp: (j, 3 * (pg * g + pp) + 1)))
        qkv_specs.append(pl.BlockSpec((BQ, LANES), lambda g, j, pp=pp: (j, 3 * (pg * g + pp) + 2)))
    pair = pl.BlockSpec((pg, S, 2), lambda g, j: (g, 0, 0))
    res = pl.pallas_call(
        body, name="fox_bwd", grid=(NPAIR // pg, NQ),
        in_specs=qkv_specs + [pl.BlockSpec((S, pg * LANES), lambda g, j: (0, g)), pl.BlockSpec((S, LANES), lambda g, j: (0, 0)),
                              pair, pair] + ex_in_specs,
        out_specs=[pl.BlockSpec((S, 3 * pg * LANES), lambda g, j: (0, g)), pl.BlockSpec((S, LANES), lambda g, j: (0, 0))]
        + ex_out_specs,
        out_shape=[jax.ShapeDtypeStruct((S, DPROJ_PAD), BF16), jax.ShapeDtypeStruct((S, LANES), F32)] + ex_out_shape,
        scratch_shapes=[pltpu.VMEM((2 * pg, S, LANES), BF16), pltpu.VMEM((pg, S, LANES), F32)] + ex_scratch,
        compiler_params=_cparams(),
    )(*([proj] * (3 * pg)), do, gates, lse, delta, *ex_in)
    return res[0], res[1], res[2:]


NQKV = 3 * NGH
GDN_QSCALE = GHD ** -0.5


def _shift_down(x, s):
    if s == 0:
        return x
    row = lax.broadcasted_iota(jnp.int32, x.shape, 0)
    return jnp.where(row >= s, pltpu.roll(x, s, 0), 0.0)


def _shift_up(x, s):
    if s == 0:
        return x
    n = x.shape[0]
    row = lax.broadcasted_iota(jnp.int32, x.shape, 0)
    return jnp.where(row < n - s, pltpu.roll(x, n - s, 0), 0.0)


def _conv_pre(xv, wv):
    pre = xv * wv[CONV_K - 1:CONV_K, :]
    for j in range(CONV_K - 1):
        pre = pre + _shift_down(xv, CONV_K - 1 - j) * wv[j:j + 1, :]
    return pre


def _l2_factors(b):
    return b < 2 * NGH, jnp.where(b < NGH, GDN_QSCALE, 1.0)


def _gdn_pre(proj, conv_w):
    def body(x_ref, w_ref, o_ref):
        b = pl.program_id(0)
        c = _silu(_conv_pre(x_ref[...], w_ref[...]))
        normed, scale = _l2_factors(b)
        rs = lax.rsqrt(jnp.sum(c * c, axis=-1, keepdims=True) + EPS)
        o_ref[...] = c * jnp.where(normed, rs, 1.0) * scale

    return pl.pallas_call(
        body, name="gdn_pre", grid=(NQKV,),
        in_specs=[pl.BlockSpec((S, GHD), lambda b: (0, BLK_GDN + b)), pl.BlockSpec((CONV_K, GHD), lambda b: (0, b))],
        out_specs=pl.BlockSpec((S, GHD), lambda b: (0, b)),
        out_shape=jax.ShapeDtypeStruct((S, NQKV * GHD), F32), compiler_params=_cparams(),
    )(proj, conv_w)


def _gdn_pre_bwd(proj, conv_w, dqkv, dproj):
    def body(x_ref, w_ref, dy_ref, dproj_in, dx_ref, dw_ref):
        del dproj_in
        b = pl.program_id(0)
        xv = x_ref[...]
        wv = w_ref[...]
        pre = _conv_pre(xv, wv)
        sig = _sigmoid(pre)
        c = pre * sig
        normed, scale = _l2_factors(b)
        g = dy_ref[...] * scale
        rs = lax.rsqrt(jnp.sum(c * c, axis=-1, keepdims=True) + EPS)
        dc_n = rs * g - c * (rs * rs * rs) * jnp.sum(g * c, axis=-1, keepdims=True)
        dc = jnp.where(normed, dc_n, g)
        dpre = dc * sig * (1.0 + pre * (1.0 - sig))
        dx = dpre * wv[CONV_K - 1:CONV_K, :]
        for j in range(CONV_K - 1):
            dx = dx + _shift_up(dpre, CONV_K - 1 - j) * wv[j:j + 1, :]
        dx_ref[...] = dx.astype(BF16)
        for j in range(CONV_K):
            dw_ref[j:j + 1, :] = jnp.sum(dpre * _shift_down(xv, CONV_K - 1 - j), axis=0, keepdims=True)

    return pl.pallas_call(
        body, name="gdn_pre_bwd", grid=(NQKV,),
        in_specs=[pl.BlockSpec((S, GHD), lambda b: (0, BLK_GDN + b)), pl.BlockSpec((CONV_K, GHD), lambda b: (0, b)),
                  pl.BlockSpec((None, S, GHD), lambda b: (b // NGH, 0, b % NGH)), pl.BlockSpec(memory_space=pl.ANY)],
        out_specs=[pl.BlockSpec((S, GHD), lambda b: (0, BLK_GDN + b)), pl.BlockSpec((CONV_K, GHD), lambda b: (0, b))],
        out_shape=[jax.ShapeDtypeStruct((S, DPROJ_PAD), BF16), jax.ShapeDtypeStruct((CONV_K, NQKV * GHD), F32)],
        input_output_aliases={3: 0}, compiler_params=_cparams(),
    )(proj, conv_w, dqkv, dproj)


CB = 16
NCB = NCH // CB


def _chunk_prep(qs, ks, vs, gcols, bcols, t_saved=None):
    n = range(len(qs))
    r = lax.broadcasted_iota(jnp.int32, (CHUNK, CHUNK), 0)
    c = lax.broadcasted_iota(jnp.int32, (CHUNK, CHUNK), 1)
    incl = c <= r
    eye = (r == c).astype(F32)
    grow = [jnp.sum(gcols[i] * eye, axis=0, keepdims=True) for i in n]
    gc_col = [jnp.sum(jnp.where(incl, grow[i], 0.0), axis=1, keepdims=True) for i in n]
    gc_row = [jnp.sum(jnp.where(r <= c, gcols[i], 0.0), axis=0, keepdims=True) for i in n]
    decay = [jnp.exp(jnp.where(incl, gc_col[i] - gc_row[i], -jnp.inf)) for i in n]
    kb = [ks[i] * bcols[i] for i in n]
    vb = [vs[i] * bcols[i] for i in n]
    kk = [_mm_nt(kb[i], ks[i]) for i in n]
    m = [jnp.where(c < r, kk[i] * decay[i], 0.0) for i in n]
    if t_saved is None:
        t_inv = [eye - m[i] for i in n]
        p = [_dot3(m[i], m[i]) for i in n]
        for step in range(5):
            t_inv = [t_inv[i] + _dot3(t_inv[i], p[i]) for i in n]
            if step < 4:
                p = [_dot3(p[i], p[i]) for i in n]
    else:
        t_inv = [_saved_inverse(m[i], t_saved[i]) for i in n]
    egc = [jnp.exp(gc_col[i]) for i in n]
    u = [_mm_nn(t_inv[i], vb[i]) for i in n]
    w = [_mm_nn(t_inv[i], kb[i] * egc[i]) for i in n]
    qk = [_mm_nt(qs[i], ks[i]) for i in n]
    gc_last = [gc_col[i][CHUNK - 1:CHUNK, :] for i in n]
    return [(u[i], w[i], qk[i] * decay[i], qs[i] * egc[i], ks[i] * jnp.exp(gc_last[i] - gc_col[i]), jnp.exp(gc_last[i]),
             t_inv[i]) for i in n]


def _prep_specs():
    rows = CB * CHUNK
    qs = pl.BlockSpec((rows, GHD), lambda i, h: (i, h))
    ks = pl.BlockSpec((rows, GHD), lambda i, h: (i, NGH + h))
    vs = pl.BlockSpec((rows, GHD), lambda i, h: (i, 2 * NGH + h))
    gs = pl.BlockSpec((rows, LANES), lambda i, h: (i, 0))
    a_s = pl.BlockSpec((None, rows, CHUNK), lambda i, h: (h, i, 0))
    gl_s = pl.BlockSpec((None, CB, 1, LANES), lambda i, h: (h, i, 0, 0))
    return qs, ks, vs, gs, a_s, gl_s


def _gdn_prep(qkv, gates, exchange=None):
    ex_in, ex_in_specs, ex_out_specs, ex_out_shape, ex_scratch = _hosted(exchange)

    def body(*refs):
        q_ref, k_ref, v_ref, g_ref = refs[:4]
        u_ref, w_ref, qd_ref, kd_ref, a_ref, gl_ref, t_ref = refs[4 + len(ex_in):11 + len(ex_in)]
        ex_refs = refs[4:4 + len(ex_in)] + refs[11 + len(ex_in):]
        h = pl.program_id(1)

        if exchange is not None:
            @pl.when((pl.program_id(0) == 0) & (h == 0))
            def _():
                exchange.start(*exchange.split(ex_refs))

        chunks = [pl.ds(cidx * CHUNK, CHUNK) for cidx in range(CB)]
        gts = [g_ref[rows, :] for rows in chunks]
        outs = _chunk_prep([q_ref[rows, :] for rows in chunks], [k_ref[rows, :] for rows in chunks],
                           [v_ref[rows, :] for rows in chunks], [_lane_col(gt, LANE_G + h) for gt in gts],
                           [_lane_col(gt, LANE_BETA + h) for gt in gts])
        for cidx, rows in enumerate(chunks):
            u, w, a, qd, kd, gl, t_inv = outs[cidx]
            u_ref[rows, :] = u
            w_ref[rows, :] = w
            qd_ref[rows, :] = qd
            kd_ref[rows, :] = kd
            a_ref[rows, :] = a
            t_ref[rows, :] = t_inv
            gl_ref[cidx] = jnp.broadcast_to(gl, (1, LANES))

        if exchange is not None:
            @pl.when((pl.program_id(0) == NCB - 1) & (h == NGH - 1))
            def _():
                exchange.finish(*exchange.split(ex_refs))

    qs, ks, vs, gs, a_s, gl_s = _prep_specs()
    tok = jax.ShapeDtypeStruct((S, DGDN), F32)
    sq = jax.ShapeDtypeStruct((NGH, S, CHUNK), F32)
    res = pl.pallas_call(
        body, name="gdn_prep", grid=(NCB, NGH), in_specs=[qs, ks, vs, gs] + ex_in_specs,
        out_specs=[qs, qs, qs, qs, a_s, gl_s, a_s] + ex_out_specs,
        out_shape=[tok, tok, tok, tok, sq, jax.ShapeDtypeStruct((NGH, NCH, 1, LANES), F32), sq] + ex_out_shape,
        scratch_shapes=ex_scratch, compiler_params=_cparams(),
    )(qkv, qkv, qkv, gates, *ex_in)
    return res[:7], res[7:]


def _gdn_prep_bwd(qkv, gates, t_inv, du, dw, dqd, dkd, da, dgl, exchange=None):
    ex_in, ex_in_specs, ex_out_specs, ex_out_shape, ex_scratch = _hosted(exchange)

    def body(*refs):
        q_ref, k_ref, v_ref, g_ref, t_ref, du_ref, dw_ref, dqd_ref, dkd_ref, da_ref, dgl_ref = refs[:11]
        dqkv_ref, dg_ref = refs[11 + len(ex_in):13 + len(ex_in)]
        ex_refs = refs[11:11 + len(ex_in)] + refs[13 + len(ex_in):]
        h = pl.program_id(1)

        if exchange is not None:
            @pl.when((pl.program_id(0) == 0) & (h == 0))
            def _():
                exchange.start(*exchange.split(ex_refs))

        @pl.when(h == 0)
        def _():
            dg_ref[...] = jnp.zeros_like(dg_ref)

        lane = lax.broadcasted_iota(jnp.int32, (CHUNK, LANES), 1)
        chunks = [pl.ds(cidx * CHUNK, CHUNK) for cidx in range(CB)]
        gts = [g_ref[rows, :] for rows in chunks]
        t_saved = [t_ref[rows, :] for rows in chunks]
        _, vjp = jax.vjp(lambda *args: [o[:6] for o in _chunk_prep(*args, t_saved=t_saved)],
                         [q_ref[rows, :] for rows in chunks], [k_ref[rows, :] for rows in chunks],
                         [v_ref[rows, :] for rows in chunks], [_lane_col(gt, LANE_G + h) for gt in gts],
                         [_lane_col(gt, LANE_BETA + h) for gt in gts])
        dqs, dks, dvs, dgcs, dbcs = vjp([(du_ref[rows, :], dw_ref[rows, :], da_ref[rows, :], dqd_ref[rows, :],
                                          dkd_ref[rows, :], dgl_ref[cidx][:, 0:1]) for cidx, rows in enumerate(chunks)])
        for cidx, rows in enumerate(chunks):
            dq, dk, dv, dgc, dbc = dqs[cidx], dks[cidx], dvs[cidx], dgcs[cidx], dbcs[cidx]
            dqkv_ref[0, rows, :] = dq
            dqkv_ref[1, rows, :] = dk
            dqkv_ref[2, rows, :] = dv
            dg_ref[rows, :] += jnp.where(lane == LANE_G + h, dgc, 0.0) + jnp.where(lane == LANE_BETA + h, dbc, 0.0)

        if exchange is not None:
            @pl.when((pl.program_id(0) == NCB - 1) & (h == NGH - 1))
            def _():
                exchange.finish(*exchange.split(ex_refs))

    qs, ks, vs, gs, a_s, gl_s = _prep_specs()
    res = pl.pallas_call(
        body, name="gdn_prep_bwd", grid=(NCB, NGH), in_specs=[qs, ks, vs, gs, a_s, qs, qs, qs, qs, a_s, gl_s] + ex_in_specs,
        out_specs=[pl.BlockSpec((3, CB * CHUNK, GHD), lambda i, h: (0, i, h)), gs] + ex_out_specs,
        out_shape=[jax.ShapeDtypeStruct((3, S, DGDN), F32), jax.ShapeDtypeStruct((S, LANES), F32)] + ex_out_shape,
        scratch_shapes=ex_scratch, compiler_params=_cparams(),
    )(qkv, qkv, qkv, gates, t_inv, du, dw, dqd, dkd, da, dgl, *ex_in)
    return res[0], res[1], res[2:]


def _scan_specs(nh, parts, reverse):
    wide, rows, chunks = nh * GHD, S // parts, NCH // parts

    def part(p):
        return parts - 1 - p if reverse else p

    hs = pl.BlockSpec((rows, wide), lambda g, p: (part(p), g))
    a_s = pl.BlockSpec((nh, rows, CHUNK), lambda g, p: (g, part(p), 0))
    gl_s = pl.BlockSpec((nh, chunks, 1, LANES), lambda g, p: (g, part(p), 0, 0))
    st_s = pl.BlockSpec((nh, chunks, GHD, GHD), lambda g, p: (g, part(p), 0, 0))
    gz_s = pl.BlockSpec((rows, wide), lambda g, p: (part(p), BLK_GZ // nh + g))
    mix_s = pl.BlockSpec((rows, wide), lambda g, p: (part(p), NPAIR // nh + g))
    return hs, a_s, gl_s, st_s, gz_s, mix_s


def _head_cols(hh):
    return slice(hh * GHD, (hh + 1) * GHD)


SCAN_HEADS, SCAN_PARTS = 4, 2
SCAN_HEADS_BWD, SCAN_PARTS_BWD = 2, 2


def _gdn_scan(u, w, qd, kd, a, gl, proj, w_norm, mix):
    heads = range(SCAN_HEADS)

    def body(u_ref, w_ref, qd_ref, kd_ref, a_ref, gl_ref, z_ref, wn_ref, mix_in, mix_ref, o_ref, st_ref, carry_ref):
        del mix_in

        @pl.when(pl.program_id(1) == 0)
        def _():
            carry_ref[...] = jnp.zeros_like(carry_ref)

        def step(ci, states):
            rows = pl.ds(pl.multiple_of(ci * CHUNK, CHUNK), CHUNK)
            for hh in heads:
                st_ref[hh, ci] = states[hh]
            ws = [_dot(w_ref[rows, _head_cols(hh)], states[hh]) for hh in heads]
            qs = [_dot(qd_ref[rows, _head_cols(hh)], states[hh]) for hh in heads]
            vn = [u_ref[rows, _head_cols(hh)] - ws[hh] for hh in heads]
            av = [_dot(a_ref[hh, rows, :], vn[hh]) for hh in heads]
            kv = [_dot(kd_ref[rows, _head_cols(hh)], vn[hh], 0, 0) for hh in heads]
            for hh in heads:
                o_ref[rows, _head_cols(hh)] = qs[hh] + av[hh]
            return tuple(states[hh] * gl_ref[hh, ci] + kv[hh] for hh in heads)

        last = lax.fori_loop(0, NCH // SCAN_PARTS, step, tuple(carry_ref[hh] for hh in heads))
        for hh in heads:
            carry_ref[hh] = last[hh]
            ov = o_ref[:, _head_cols(hh)]
            mix_ref[:, _head_cols(hh)] = (ov * _rms_scale(ov) * wn_ref[...] * _silu(z_ref[:, _head_cols(hh)])).astype(BF16)

    hs, a_s, gl_s, st_s, gz_s, mix_s = _scan_specs(SCAN_HEADS, SCAN_PARTS, False)
    return pl.pallas_call(
        body, name="gdn_scan", grid=(NGH // SCAN_HEADS, SCAN_PARTS),
        in_specs=[hs, hs, hs, hs, a_s, gl_s, gz_s, pl.BlockSpec((1, GHD), lambda g, p: (0, 0)),
                  pl.BlockSpec(memory_space=pl.ANY)],
        out_specs=[mix_s, hs, st_s],
        out_shape=[jax.ShapeDtypeStruct((S, D), BF16), jax.ShapeDtypeStruct((S, DGDN), F32),
                   jax.ShapeDtypeStruct((NGH, NCH, GHD, GHD), F32)],
        input_output_aliases={8: 0}, scratch_shapes=[pltpu.VMEM((SCAN_HEADS, GHD, GHD), F32)], compiler_params=_cparams(),
    )(u, w, qd, kd, a, gl, proj, w_norm, mix)


def _gdn_scan_bwd(dmix, o, proj, w_norm, u, w, qd, kd, a, gl, states, dproj, exchange=None):
    ex_in, ex_in_specs, ex_out_specs, ex_out_shape, ex_scratch = _hosted(exchange)
    groups = NGH // SCAN_HEADS_BWD

    def body(*refs):
        dy_ref, o_ref, z_ref, wn_ref, u_ref, w_ref, qd_ref, kd_ref, a_ref, gl_ref, st_ref = refs[:11]
        dz_ref, du_ref, dw_ref, dqd_ref, dkd_ref, da_ref, dgl_ref, dwn_ref = refs[12 + len(ex_in):20 + len(ex_in)]
        do_ref, carry_ref = refs[20 + len(ex_in) + len(ex_out_shape):22 + len(ex_in) + len(ex_out_shape)]
        ex_refs = refs[12:12 + len(ex_in)] + refs[20 + len(ex_in):20 + len(ex_in) + len(ex_out_shape)] + refs[-2:]
        heads = range(SCAN_HEADS_BWD)
        chunks = NCH // SCAN_PARTS_BWD

        if exchange is not None:
            @pl.when((pl.program_id(0) == 0) & (pl.program_id(1) == 0))
            def _():
                exchange.start(*exchange.split(ex_refs))

        @pl.when((pl.program_id(0) == 0) & (pl.program_id(1) == 0))
        def _():
            dwn_ref[...] = jnp.zeros_like(dwn_ref)

        @pl.when(pl.program_id(1) == 0)
        def _():
            carry_ref[...] = jnp.zeros_like(carry_ref)

        wn = wn_ref[...]
        for hh in heads:
            c = _head_cols(hh)
            ov = o_ref[:, c]
            zv = z_ref[:, c]
            g = dy_ref[:, c]
            sig = _sigmoid(zv)
            dz_ref[:, c] = (g * (ov * _rms_scale(ov) * wn) * sig * (1.0 + zv * (1.0 - sig))).astype(BF16)
            do, dwt = _rms_bwd(ov, wn, g * zv * sig)
            do_ref[:, c] = do
            dwn_ref[...] += jnp.sum(dwt, axis=0, keepdims=True)

        def step(t, dstates):
            ci = chunks - 1 - t
            rows = pl.ds(pl.multiple_of(ci * CHUNK, CHUNK), CHUNK)
            cols = [_head_cols(hh) for hh in heads]
            state = [st_ref[hh, ci] for hh in heads]
            dov = [do_ref[rows, cols[hh]] for hh in heads]
            wv = [w_ref[rows, cols[hh]] for hh in heads]
            ws = [_dot(wv[hh], state[hh]) for hh in heads]
            adov = [_dot(a_ref[hh, rows, :], dov[hh], 0, 0) for hh in heads]
            kds = [_dot(kd_ref[rows, cols[hh]], dstates[hh]) for hh in heads]
            dqd = [_dot(dov[hh], state[hh], 1, 1) for hh in heads]
            qdo = [_dot(qd_ref[rows, cols[hh]], dov[hh], 0, 0) for hh in heads]
            vn = [u_ref[rows, cols[hh]] - ws[hh] for hh in heads]
            dvn = [adov[hh] + kds[hh] for hh in heads]
            da = [_dot(dov[hh], vn[hh], 1, 1) for hh in heads]
            dkd = [_dot(vn[hh], dstates[hh], 1, 1) for hh in heads]
            dwv = [_dot(dvn[hh], state[hh], 1, 1) for hh in heads]
            wdv = [_dot(wv[hh], dvn[hh], 0, 0) for hh in heads]
            for hh in heads:
                da_ref[hh, rows, :] = da[hh]
                dqd_ref[rows, cols[hh]] = dqd[hh]
                dkd_ref[rows, cols[hh]] = dkd[hh]
                dgl = jnp.sum(jnp.sum(dstates[hh] * state[hh], axis=1, keepdims=True), axis=0, keepdims=True)
                dgl_ref[hh, ci] = jnp.broadcast_to(dgl, (1, LANES))
                du_ref[rows, cols[hh]] = dvn[hh]
                dw_ref[rows, cols[hh]] = -dwv[hh]
            return tuple(dstates[hh] * gl_ref[hh, ci] + qdo[hh] - wdv[hh] for hh in heads)

        last = lax.fori_loop(0, chunks, step, tuple(carry_ref[hh] for hh in heads))
        for hh in heads:
            carry_ref[hh] = last[hh]

        if exchange is not None:
            @pl.when((pl.program_id(0) == groups - 1) & (pl.program_id(1) == SCAN_PARTS_BWD - 1))
            def _():
                exchange.finish(*exchange.split(ex_refs))

    hs, a_s, gl_s, st_s, gz_s, mix_s = _scan_specs(SCAN_HEADS_BWD, SCAN_PARTS_BWD, True)
    vec = pl.BlockSpec((1, GHD), lambda g, p: (0, 0))
    tok = jax.ShapeDtypeStruct((S, DGDN), F32)
    res = pl.pallas_call(
        body, name="gdn_scan_bwd", grid=(groups, SCAN_PARTS_BWD),
        in_specs=[mix_s, hs, gz_s, vec, hs, hs, hs, hs, a_s, gl_s, st_s, pl.BlockSpec(memory_space=pl.ANY)] + ex_in_specs,
        out_specs=[gz_s, hs, hs, hs, hs, a_s, gl_s, vec] + ex_out_specs,
        out_shape=[jax.ShapeDtypeStruct((S, DPROJ_PAD), BF16), tok, tok, tok, tok,
                   jax.ShapeDtypeStruct((NGH, S, CHUNK), F32), jax.ShapeDtypeStruct((NGH, NCH, 1, LANES), F32),
                   jax.ShapeDtypeStruct((1, GHD), F32)] + ex_out_shape,
        input_output_aliases={11: 0},
        scratch_shapes=[pltpu.VMEM((S // SCAN_PARTS_BWD, SCAN_HEADS_BWD * GHD), F32),
                        pltpu.VMEM((SCAN_HEADS_BWD, GHD, GHD), F32)] + ex_scratch,
        compiler_params=_cparams(),
    )(dmix, o, proj, w_norm, u, w, qd, kd, a, gl, states, dproj, *ex_in)
    return res[:8], res[8:]


def _place():
    return lax.axis_index("x"), lax.axis_index("y"), lax.axis_index("c")


def _other_chips(x, y):
    return [(1 - x, y), (x, 1 - y), (1 - x, 1 - y)]


HBM = pl.BlockSpec(memory_space=pltpu.HBM)
VMEM = pl.BlockSpec(memory_space=pltpu.VMEM)


def _half_rows(ref_or_rows, half):
    rows = ref_or_rows // 2
    return pl.ds(pl.multiple_of(half * rows, rows), rows)


class _Exchange:
    def __init__(self, inputs, out_shape, n_sems, start, finish):
        self.inputs, self.out_shape, self.n_sems, self.start, self.finish = inputs, out_shape, n_sems, start, finish

    def sem_shapes(self):
        return [pltpu.SemaphoreType.DMA((self.n_sems,)), pltpu.SemaphoreType.DMA((self.n_sems,))]

    def split(self, refs):
        n_in, n_out = len(self.inputs), len(self.out_shape)
        return refs[:n_in], refs[n_in:n_in + n_out], refs[n_in + n_out], refs[n_in + n_out + 1]


def _run_exchange(ex, name):
    def body(*refs):
        parts = ex.split(refs)
        ex.start(*parts)
        ex.finish(*parts)

    return pl.pallas_call(
        body, name=name, in_specs=[HBM] * len(ex.inputs), out_specs=[HBM] * len(ex.out_shape), out_shape=ex.out_shape,
        scratch_shapes=ex.sem_shapes(), compiler_params=_cparams(),
    )(*ex.inputs)


def _allgather_exchange(shards, whole=()):
    n, nw = len(shards), len(whole)

    def plan(src, outs, send_sems, recv_sems):
        x, y, c = _place()
        chips = _other_chips(x, y)
        chip_ids = [2 * ch[0] + ch[1] for ch in chips]

        def copy(a, k, chip_index, half, to, from_src):
            rows = _half_rows(src[a].shape[0], half)
            dst = outs[a].at[chip_index, rows]
            return pltpu.make_async_remote_copy(
                src_ref=src[a].at[rows] if from_src else dst, dst_ref=dst, send_sem=send_sems.at[6 * a + k],
                recv_sem=recv_sems.at[6 * a + k], device_id=to, device_id_type=MESH)

        def whole_copy(b, k, chip_index, to):
            return pltpu.make_async_remote_copy(
                src_ref=src[n + b], dst_ref=outs[n + b].at[chip_index], send_sem=send_sems.at[6 * n + 3 * b + k],
                recv_sem=recv_sems.at[6 * n + 3 * b + k], device_id=to, device_id_type=MESH)

        me, sibling = (x, y, c), (x, y, 1 - c)
        first = [copy(a, j, 2 * x + y, c, (*chips[j], c), True) for a in range(n) for j in range(3)]
        first += [whole_copy(b, j, 2 * x + y, (*chips[j], c)) for b in range(nw) for j in range(3)]
        landing = [copy(a, j, chip_ids[j], c, me, False) for a in range(n) for j in range(3)]
        passed = [copy(a, 3 + j, chip_ids[j], c, sibling, False) for a in range(n) for j in range(3)]
        arriving = [copy(a, 3 + j, chip_ids[j], 1 - c, me, False) for a in range(n) for j in range(3)]
        arriving += [whole_copy(b, j, chip_ids[j], me) for b in range(nw) for j in range(3)]
        return first, landing, passed, arriving

    def start(*refs):
        for cp in plan(*refs)[0]:
            cp.start()

    def finish(*refs):
        first, landing, passed, arriving = plan(*refs)
        for lands, onward in zip(landing, passed):
            lands.wait_recv()
            onward.start()
        for cp in arriving:
            cp.wait_recv()
        for cp in first + passed:
            cp.wait_send()

    out_shape = [jax.ShapeDtypeStruct((NCHIP,) + s.shape, s.dtype) for s in list(shards) + list(whole)]
    return _Exchange(list(shards) + list(whole), out_shape, 6 * n + 3 * nw, start, finish)


def _with_own(gathered, own):
    x, y, _ = _place()
    return lax.dynamic_update_index_in_dim(gathered, own, 2 * x + y, axis=0)


def _simple_exchange(inputs, out_shape, copies_of):
    def start(*refs):
        for cp in copies_of(*refs):
            cp.start()

    def finish(*refs):
        for cp in copies_of(*refs):
            cp.wait()

    return _Exchange(list(inputs), out_shape, len(out_shape) * 3, start, finish)


def _pair_exchange(grads):
    def copies_of(src, outs, send_sems, recv_sems):
        x, y, c = _place()
        return [pltpu.make_async_remote_copy(
            src_ref=src[a].at[:, _half_rows(src[a].shape[1], 1 - c)], dst_ref=outs[a], send_sem=send_sems.at[a],
            recv_sem=recv_sems.at[a], device_id=(x, y, 1 - c), device_id_type=MESH) for a in range(len(src))]

    return _simple_exchange(
        grads, [jax.ShapeDtypeStruct((g.shape[0], g.shape[1] // 2, g.shape[2]), g.dtype) for g in grads], copies_of)


def _pair_sum(grads, theirs, name):
    n = len(grads)

    def body(*refs):
        south = lax.axis_index("c") == 0
        for a in range(n):
            g = refs[a][...]
            half = g.shape[0] // 2
            mine = jnp.where(south, g[:half], g[half:])
            refs[2 * n + a][...] = (mine.astype(F32) + refs[n + a][...].astype(F32)).astype(BF16)

    def specs(arrs):
        return [pl.BlockSpec((None,) + g.shape[1:], lambda j: (j, 0, 0)) for g in arrs]

    return pl.pallas_call(
        body, name=name, grid=(NCHIP,), in_specs=specs(grads) + specs(theirs), out_specs=specs(theirs),
        out_shape=[jax.ShapeDtypeStruct(g.shape, BF16) for g in theirs], compiler_params=_cparams(),
    )(*grads, *theirs)


def _chip_exchange(parts):
    def copies_of(src, outs, send_sems, recv_sems):
        x, y, c = _place()
        return [pltpu.make_async_remote_copy(
            src_ref=src[a].at[2 * chip[0] + chip[1]], dst_ref=outs[a].at[k], send_sem=send_sems.at[3 * a + k],
            recv_sem=recv_sems.at[3 * a + k], device_id=(*chip, c), device_id_type=MESH)
            for a in range(len(src)) for k, chip in enumerate(_other_chips(x, y))]

    return _simple_exchange(parts, [jax.ShapeDtypeStruct((NCHIP - 1,) + p.shape[1:], p.dtype) for p in parts], copies_of)


def _chip_sum(parts, received, exchange=None):
    n = len(parts)
    steps = 4
    ex_in, ex_in_specs, ex_out_specs, ex_out_shape, ex_scratch = _hosted(exchange)

    def body(*refs):
        ex_refs = refs[2 * n:2 * n + len(ex_in)] + refs[3 * n + len(ex_in):]
        if exchange is not None:
            @pl.when(pl.program_id(0) == 0)
            def _():
                exchange.start(*exchange.split(ex_refs))

        chip = 2 * lax.axis_index("x") + lax.axis_index("y")
        for a in range(n):
            p, r = refs[a], refs[n + a]
            own = jnp.where(chip == 0, p[0], jnp.where(chip == 1, p[1], jnp.where(chip == 2, p[2], p[3])))
            refs[2 * n + len(ex_in) + a][...] = ((own.astype(F32) + r[0].astype(F32)) + r[1].astype(F32)) + r[2].astype(F32)

        if exchange is not None:
            @pl.when(pl.program_id(0) == steps - 1)
            def _():
                exchange.finish(*exchange.split(ex_refs))

    def specs(arrs):
        return [pl.BlockSpec((g.shape[0], g.shape[1] // steps, g.shape[2]), lambda i: (0, i, 0)) for g in arrs]

    out_specs = [pl.BlockSpec((g.shape[1] // steps, g.shape[2]), lambda i: (i, 0)) for g in parts]
    res = pl.pallas_call(
        body, name="grads_chip_sum", grid=(steps,), in_specs=specs(parts) + specs(received) + ex_in_specs,
        out_specs=out_specs + ex_out_specs, out_shape=[jax.ShapeDtypeStruct(g.shape[1:], F32) for g in parts] + ex_out_shape,
        scratch_shapes=ex_scratch, compiler_params=_cparams(),
    )(*parts, *received, *ex_in)
    return res[:n], res[n:]


def _pair_share(halves):
    def copies_of(src, outs, send_sems, recv_sems):
        x, y, c = _place()
        return [pltpu.make_async_remote_copy(
            src_ref=src[a], dst_ref=outs[a], send_sem=send_sems.at[a], recv_sem=recv_sems.at[a],
            device_id=(x, y, 1 - c), device_id_type=MESH) for a in range(len(src))]

    return _simple_exchange(halves, [jax.ShapeDtypeStruct(h.shape, F32) for h in halves], copies_of)


def _adamw_math(w, g, m, v):
    nm = ADAM_B1 * m + (1.0 - ADAM_B1) * g
    nv = ADAM_B2 * v + (1.0 - ADAM_B2) * jnp.square(g)
    m_hat = nm / (1.0 - ADAM_B1 ** ADAM_STEP)
    v_hat = nv / (1.0 - ADAM_B2 ** ADAM_STEP)
    return -ADAM_LR * (m_hat / (jnp.sqrt(v_hat) + ADAM_EPS) + ADAM_WD * w), nm, nv


def _adamw_big(ws, g_mine, g_theirs, ms, vs, exchange=None):
    n = len(ws)
    steps = 8
    ex_in, ex_in_specs, ex_out_specs, ex_out_shape, ex_scratch = _hosted(exchange)

    def body(*refs):
        ex_refs = refs[5 * n:5 * n + len(ex_in)] + refs[9 * n + len(ex_in):]
        outs = refs[5 * n + len(ex_in):9 * n + len(ex_in)]
        if exchange is not None:
            @pl.when(pl.program_id(0) == 0)
            def _():
                exchange.start(*exchange.split(ex_refs))

        own_half = (pl.program_id(0) // (steps // 2)) == lax.axis_index("c")
        for a in range(n):
            g = jnp.where(own_half, refs[n + a][...], refs[2 * n + a][...])
            d, nm, nv = _adamw_math(refs[a][...], g, refs[3 * n + a][...], refs[4 * n + a][...])
            outs[a][...] = g
            outs[n + a][...] = d
            outs[2 * n + a][...] = nm
            outs[3 * n + a][...] = nv

        if exchange is not None:
            @pl.when(pl.program_id(0) == steps - 1)
            def _():
                exchange.finish(*exchange.split(ex_refs))

    specs = [pl.BlockSpec((w.shape[0] // steps, w.shape[1]), lambda i: (i, 0)) for w in ws]
    half_specs = [pl.BlockSpec((g.shape[0] // (steps // 2), g.shape[1]), lambda i: (i % (steps // 2), 0)) for g in g_mine]
    shapes = [jax.ShapeDtypeStruct(w.shape, F32) for w in ws]
    res = pl.pallas_call(
        body, name="adamw_big", grid=(steps,), in_specs=specs + half_specs * 2 + specs * 2 + ex_in_specs,
        out_specs=specs * 4 + ex_out_specs, out_shape=shapes * 4 + ex_out_shape, scratch_shapes=ex_scratch,
        compiler_params=_cparams(),
    )(*ws, *g_mine, *g_theirs, *ms, *vs, *ex_in)
    return res[:n], res[n:2 * n], res[2 * n:3 * n], res[3 * n:4 * n], res[4 * n:]


def _adamw_in(w, g_mine, g_theirs, m, v):
    half = D // 2

    def body(w_ref, gm_ref, gt_ref, m_ref, v_ref, g_out, d_out, nm_out, nv_out, g_ref):
        south = lax.axis_index("c") == 0
        g_ref[0:half, :] = jnp.where(south, gm_ref[...], gt_ref[...])
        g_ref[half:D, :] = jnp.where(south, gt_ref[...], gm_ref[...])
        g = g_ref[0:CW, :]
        d, nm, nv = _adamw_math(w_ref[...], g, m_ref[...], v_ref[...])
        g_out[...] = g
        d_out[...] = d
        nm_out[...] = nm
        nv_out[...] = nv

    spec = pl.BlockSpec((CW, LANES), lambda i: (0, i))
    half_spec = pl.BlockSpec((half, LANES), lambda i: (0, i))
    return pl.pallas_call(
        body, name="adamw_in", grid=(D // LANES,), in_specs=[spec, half_spec, half_spec, spec, spec], out_specs=[spec] * 4,
        out_shape=[jax.ShapeDtypeStruct((CW, D), F32)] * 4, scratch_shapes=[pltpu.VMEM((D, LANES), F32)],
        compiler_params=_cparams(),
    )(w, g_mine, g_theirs, m, v)


NORM_NAMES = ("pre_mix_norm", "post_mix_norm", "pre_mlp_norm", "post_mlp_norm")
SMALL_NAMES = NORM_NAMES + ("gdn_conv_w", "fox_f_bias", "gdn_dt_bias", "gdn_a_log", "fox_out_norm", "gdn_out_norm")
CONV_COLS = 3 * DGDN // NCHIP


def _small_gather(d_norms, d_conv, sums, d_fox_norm, d_gdn_norm, loss_row):
    n_arrays = 6
    n_remote = n_arrays * (NDEV - 1)

    def copies_of(src, outs, send_sems, recv_sems):
        x, y, c = _place()
        me = 4 * x + 2 * y + c

        def from_me(chip_index):
            cols = pl.ds(pl.multiple_of(chip_index * CONV_COLS, LANES), CONV_COLS)
            return [src[0], src[1].at[:, cols], src[2], src[3], src[4], src[5]]

        local = [pltpu.make_async_copy(s, outs[a].at[me], send_sems.at[n_remote + a]) for a, s in enumerate(from_me(2 * x + y))]
        remote = []
        for k in range(1, NDEV):
            px, py, pc = x ^ ((k >> 2) & 1), y ^ ((k >> 1) & 1), c ^ (k & 1)
            remote += [pltpu.make_async_remote_copy(
                src_ref=s, dst_ref=outs[a].at[me], send_sem=send_sems.at[n_arrays * (k - 1) + a],
                recv_sem=recv_sems.at[n_arrays * (k - 1) + a], device_id=(px, py, pc), device_id_type=MESH)
                for a, s in enumerate(from_me(2 * px + py))]
        return local + remote

    def start(*refs):
        for cp in copies_of(*refs):
            cp.start()

    def finish(*refs):
        for cp in copies_of(*refs):
            cp.wait()

    shapes = [(4, D), (CONV_K, CONV_COLS), (8, LANES), (1, LANES), (1, LANES), (1, LANES)]
    return _Exchange([d_norms, d_conv, sums, d_fox_norm, d_gdn_norm, loss_row],
                     [jax.ShapeDtypeStruct((NDEV,) + s, F32) for s in shapes], n_remote + n_arrays, start, finish)


def _small_adamw(gathered, ws, ms, vs):
    n = len(SMALL_NAMES)
    ng = len(gathered)

    def body(*refs):
        def total(buf):
            acc = buf[0]
            for i in range(1, NDEV):
                acc = acc + buf[i]
            return acc

        t_norms, t_conv, t_sums, t_fn, t_gn, t_loss = [total(r) for r in refs[:ng]]
        w_refs, m_refs, v_refs = refs[ng:ng + n], refs[ng + n:ng + 2 * n], refs[ng + 2 * n:ng + 3 * n]
        outs = refs[ng + 3 * n:]
        outs[4 * n][...] = t_loss
        grads = [t_norms[i:i + 1, :] for i in range(4)] + [
            t_conv, t_sums[0:1, 0:NFH], t_sums[1:2, 0:NGH], t_sums[2:3, 0:NGH], t_fn[:, 0:FHD], t_gn]
        for a in range(n):
            d, nm, nv = _adamw_math(w_refs[a][...], grads[a], m_refs[a][...], v_refs[a][...])
            outs[a][...] = grads[a]
            outs[n + a][...] = d
            outs[2 * n + a][...] = nm
            outs[3 * n + a][...] = nv

    def whole(arr):
        return pl.BlockSpec(arr.shape, lambda i: (0,) * arr.ndim)

    res = pl.pallas_call(
        body, name="small_adamw", grid=(1,), in_specs=[whole(t) for t in gathered] + [whole(w) for w in ws] * 3,
        out_specs=[whole(w) for w in ws] * 4 + [pl.BlockSpec((1, LANES), lambda i: (0, 0))],
        out_shape=[jax.ShapeDtypeStruct(w.shape, F32) for w in ws] * 4 + [jax.ShapeDtypeStruct((1, LANES), F32)],
        compiler_params=_cparams(),
    )(*gathered, *ws, *ms, *vs)
    return res[:n], res[n:2 * n], res[2 * n:3 * n], res[3 * n:4 * n], res[4 * n]


CW = DPROJ // NCHIP
PROJ_RUNS = tuple((part * DFOX + hp * LANES, part * DFOX + (hp + 1) * LANES, (3 * hp + part) * LANES)
                  for hp in range(NPAIR) for part in range(3)) + (
    (1536, 1544, BLK_SMALL * LANES), (1544, 3080, BLK_GDN * LANES), (3080, 3088, BLK_SMALL * LANES + 8),
    (3088, 3600, BLK_GZ * LANES))


def _proj_pieces():
    pieces = []
    for lo, hi, at in PROJ_RUNS:
        while lo < hi:
            j = lo // CW
            end = min(hi, (j + 1) * CW)
            pieces.append((j, lo - j * CW, at, end - lo))
            at, lo = at + end - lo, end
    return pieces


RT = 256


def _to_padded_rows(gathered):
    def body(src_ref, out_ref, blocks_ref, rows_ref):
        blocks_ref[...] = src_ref[...].astype(F32)
        rows_ref[...] = jnp.zeros_like(rows_ref)
        for j, start, at, n in _proj_pieces():
            rows_ref[at:at + n, :] = blocks_ref[j, start:start + n, :]
        out_ref[...] = rows_ref[...].astype(out_ref.dtype)

    return pl.pallas_call(
        body, name="proj_rows_in", grid=(D // RT,), in_specs=[pl.BlockSpec((NCHIP, D, RT), lambda i: (0, 0, i))],
        out_specs=pl.BlockSpec((DPROJ_PAD, RT), lambda i: (0, i)), out_shape=jax.ShapeDtypeStruct((DPROJ_PAD, D), gathered.dtype),
        scratch_shapes=[pltpu.VMEM((NCHIP, D, RT), F32), pltpu.VMEM((DPROJ_PAD, RT), F32)], compiler_params=_cparams(),
    )(gathered)


def _from_padded_rows(w):
    def body(src_ref, out_ref, rows_ref, blocks_ref):
        rows_ref[...] = src_ref[...].astype(F32)
        blocks_ref[...] = jnp.zeros_like(blocks_ref)
        for j, start, at, n in _proj_pieces():
            blocks_ref[j, start:start + n, :] = rows_ref[at:at + n, :]
        out_ref[...] = blocks_ref[...].astype(out_ref.dtype)

    return pl.pallas_call(
        body, name="proj_rows_out", grid=(D // RT,), in_specs=[pl.BlockSpec((DPROJ_PAD, RT), lambda i: (0, i))],
        out_specs=pl.BlockSpec((NCHIP, D, RT), lambda i: (0, 0, i)), out_shape=jax.ShapeDtypeStruct((NCHIP, D, D), w.dtype),
        scratch_shapes=[pltpu.VMEM((DPROJ_PAD, RT), F32), pltpu.VMEM((NCHIP, D, RT), F32)], compiler_params=_cparams(),
    )(w)


def _local_step(x, target, first_weights, late_weights, reduce_late, reduce_in, pre_mix_norm, fox_f_bias, fox_out_norm,
                gdn_a_log, gdn_dt_bias, gdn_out_norm, post_mix_norm, pre_mlp_norm, post_mlp_norm):
    bias_vec = jnp.zeros((1, LANES), F32).at[0, 0:NFH].set(fox_f_bias).at[0, LANE_G:LANE_G + NGH].set(gdn_dt_bias)
    alog_vec = jnp.zeros((1, LANES), F32).at[0, LANE_G:LANE_G + NGH].set(gdn_a_log)
    w2 = jnp.concatenate([fox_out_norm, fox_out_norm], axis=1)

    h, first = _pre_norm(x, pre_mix_norm, exchange=first_weights[0])
    win_p, conv_w = first_weights[1](first)
    proj = _matmul(h, win_p, tb=True, tm=2048, tn=768, tk=1024, name="mm_proj", exchange=late_weights[0])
    proj, late_a = proj if late_weights[0] is not None else (proj, [])
    gates = _gates(proj, bias_vec, alog_vec)
    mix, fox_o, lse, late_b = _fox_fwd(proj, gates, w2, exchange=late_weights[1])
    qkv = _gdn_pre(proj, conv_w)
    (u, w, qd, kd, a_intra, gl, t_inv), late_c = _gdn_prep(qkv, gates, exchange=late_weights[2])
    wout, wup3, wdown = late_weights[3](late_a, late_b, late_c)
    mix, gdn_raw, states = _gdn_scan(u, w, qd, kd, a_intra, gl, proj, gdn_out_norm, mix)
    mixed = _matmul(mix, wout, tm=2048, tk=1024, name="mm_out")
    x1, h2 = _post_mix(x, mixed, post_mix_norm, pre_mlp_norm)

    def relu2(acc):
        r = jnp.maximum(acc, 0.0)
        return r, r * r

    up_relu, act = _matmul(h2, wup3, b3=True, tm=1024, tn=1024, tk=1024, out_dtypes=(BF16, BF16), epilogue=relu2,
                           name="mm_up")
    y = _matmul(act, wdown, tm=1024, tk=DFF, name="mm_down")
    dx2, dy, d_post_mlp, loss_row = _loss_head(x1, y, post_mlp_norm, target)

    dwdown = _matmul(act, dy, ta=True, tm=1024, tn=1024, tk=2048, out_dtypes=(BF16,), name="mm_dwdown")

    def relu2_bwd(acc, r):
        return (acc * 2.0 * r.astype(F32),)

    dup = _matmul(dy, wdown, tb=True, tm=1024, tn=1024, tk=1024, out_dtypes=(BF16,), extra=(up_relu,), epilogue=relu2_bwd,
                  name="mm_dact")
    dwup3 = _matmul(h2, dup, ta=True, tm=1024, tn=1024, tk=2048, out_dtypes=(BF16,), o3=True, name="mm_dwup")
    dh2 = _matmul(dup, wup3, tb=True, b3=True, tm=1024, tk=DFF, name="mm_dh2")
    dx1, dmixed, d_pre_mlp, d_post_mix = _mid_bwd(dh2, x1, pre_mlp_norm, dx2, mixed, post_mix_norm)
    dwout = _matmul(mix, dmixed, ta=True, tm=1024, tn=1024, tk=2048, out_dtypes=(BF16,), name="mm_dwout")
    dmix = _matmul(dmixed, wout, tb=True, tm=2048, tk=1024, name="mm_dmix")

    dfox, delta, d_fox_norm, from_sibling = _fox_norm_bwd(fox_o, dmix, w2, exchange=reduce_late[0](dwout, dwup3, dwdown))
    dproj, dcum_fox, reduced_a = _fox_bwd(proj, dfox, gates, lse, delta, exchange=reduce_late[1](from_sibling))
    (dproj, du, dw, dqd, dkd, da, dgl, d_gdn_norm), reduced_b = _gdn_scan_bwd(
        dmix, gdn_raw, proj, gdn_out_norm, u, w, qd, kd, a_intra, gl, states, dproj, exchange=reduce_late[2]())
    dqkv, dgates_gdn, reduced_c = _gdn_prep_bwd(qkv, gates, t_inv, du, dw, dqd, dkd, da, dgl, exchange=reduce_late[3]())
    reduced_late = (reduced_a, reduced_b, reduced_c)
    dproj, d_conv = _gdn_pre_bwd(proj, conv_w, dqkv, dproj)
    dproj, sums = _gates_bwd(proj, bias_vec, alog_vec, dgates_gdn, dcum_fox, dproj)

    dwin_p = _matmul(dproj, h, ta=True, tm=1280, tn=1024, tk=2048, out_dtypes=(BF16,), name="mm_dwin")
    exchange_in = reduce_in(dwin_p)
    dh = _matmul(dproj, win_p, tm=1024, tk=DPROJ_PAD, name="mm_dh", exchange=exchange_in)
    dh, reduced_in = dh if exchange_in is not None else (dh, [])
    grad_x, d_pre_mix = _pre_norm_bwd(dh, x, pre_mix_norm, dx1)

    d_norms = jnp.concatenate([d_pre_mix, d_post_mix, d_pre_mlp, d_post_mlp], axis=0)
    return grad_x, (d_norms, d_conv, sums, d_fox_norm, d_gdn_norm, loss_row), reduced_late, reduced_in


def kernel(x, pre_mix_norm, w_in, fox_f_bias, fox_out_norm, gdn_conv_w, gdn_a_log, gdn_dt_bias, gdn_out_norm, w_out, post_mix_norm, pre_mlp_norm, w_up, w_down, post_mlp_norm, loss_target, m_pre_mix_norm, m_w_in, m_fox_f_bias, m_fox_out_norm, m_gdn_conv_w, m_gdn_a_log, m_gdn_dt_bias, m_gdn_out_norm, m_w_out, m_post_mix_norm, m_pre_mlp_norm, m_w_up, m_w_down, m_post_mlp_norm, v_pre_mix_norm, v_w_in, v_fox_f_bias, v_fox_out_norm, v_gdn_conv_w, v_gdn_a_log, v_gdn_dt_bias, v_gdn_out_norm, v_w_out, v_post_mix_norm, v_pre_mlp_norm, v_w_up, v_w_down, v_post_mlp_norm):
    weights = dict(pre_mix_norm=pre_mix_norm, w_in=w_in, fox_f_bias=fox_f_bias, fox_out_norm=fox_out_norm, gdn_conv_w=gdn_conv_w,
                   gdn_a_log=gdn_a_log, gdn_dt_bias=gdn_dt_bias, gdn_out_norm=gdn_out_norm, w_out=w_out, post_mix_norm=post_mix_norm,
                   pre_mlp_norm=pre_mlp_norm, w_up=w_up, w_down=w_down, post_mlp_norm=post_mlp_norm)
    m_in = dict(pre_mix_norm=m_pre_mix_norm, w_in=m_w_in, fox_f_bias=m_fox_f_bias, fox_out_norm=m_fox_out_norm, gdn_conv_w=m_gdn_conv_w,
                gdn_a_log=m_gdn_a_log, gdn_dt_bias=m_gdn_dt_bias, gdn_out_norm=m_gdn_out_norm, w_out=m_w_out, post_mix_norm=m_post_mix_norm,
                pre_mlp_norm=m_pre_mlp_norm, w_up=m_w_up, w_down=m_w_down, post_mlp_norm=m_post_mlp_norm)
    v_in = dict(pre_mix_norm=v_pre_mix_norm, w_in=v_w_in, fox_f_bias=v_fox_f_bias, fox_out_norm=v_fox_out_norm, gdn_conv_w=v_gdn_conv_w,
                gdn_a_log=v_gdn_a_log, gdn_dt_bias=v_gdn_dt_bias, gdn_out_norm=v_gdn_out_norm, w_out=v_w_out, post_mix_norm=v_post_mix_norm,
                pre_mlp_norm=v_pre_mlp_norm, w_up=v_w_up, w_down=v_w_down, post_mlp_norm=v_post_mlp_norm)
    order_w = ("pre_mix_norm", "w_in", "fox_f_bias", "fox_out_norm", "gdn_conv_w", "gdn_a_log", "gdn_dt_bias", "gdn_out_norm", "w_out",
               "post_mix_norm", "pre_mlp_norm", "w_up", "w_down", "post_mlp_norm")
    big = ("w_in", "w_out", "w_up", "w_down")

    def row(v):
        return v if v.ndim == 2 else v.reshape(1, -1)

    win_shard = jnp.pad(w_in.T.astype(BF16), ((0, D - CW), (0, 0)))

    def resolve_first(gathered):
        win_g, conv_g = gathered
        return (_to_padded_rows(_with_own(win_g, win_shard)),
                _with_own(conv_g, gdn_conv_w).transpose(1, 0, 2).reshape(CONV_K, 3 * DGDN))

    late_shards = [weights[n].astype(BF16) for n in big[1:]]

    def resolve_late(gathered_out, gathered_mlp, _):
        wout_g, wup3, wdown_g = [_with_own(g, own) for g, own in zip(list(gathered_out) + list(gathered_mlp), late_shards)]
        return wout_g.reshape(D, D), wup3, wdown_g.reshape(DFF, D)

    pair_sums, late_blocks = {}, []

    def pair_summed(names, blocks, theirs):
        for n, s in zip(names, _pair_sum(blocks, theirs, "grads_pair_sum_" + names[0])):
            pair_sums[n] = s

    def late_pair_exchange(dwout, dwup3, dwdown):
        late_blocks.extend([dwout.reshape(NCHIP, D // NCHIP, D), dwup3, dwdown.reshape(NCHIP, DFF // NCHIP, D)])
        return _pair_exchange(late_blocks)

    def late_chip_exchange(theirs):
        pair_summed(big[1:], late_blocks, theirs)
        return _chip_exchange([pair_sums["w_up"], pair_sums["w_down"]])

    def reduce_in(dwin_p):
        blocks = [_from_padded_rows(dwin_p)]
        pair_summed(big[:1], blocks, _run_exchange(_pair_exchange(blocks), "grads_pair_exchange_w_in"))
        return _chip_exchange([pair_sums["w_in"]])

    grad_x, small, received_late, received_in = _local_step(
        x[0], loss_target[0], (_allgather_exchange([win_shard], whole=[gdn_conv_w]), resolve_first),
        (_allgather_exchange(late_shards[:1]), _allgather_exchange(late_shards[1:]), None, resolve_late),
        (late_pair_exchange, late_chip_exchange, lambda: None, lambda: _chip_exchange([pair_sums["w_out"]])),
        reduce_in, row(pre_mix_norm), fox_f_bias, row(fox_out_norm), gdn_a_log, gdn_dt_bias,
        row(gdn_out_norm), row(post_mix_norm), row(pre_mlp_norm), row(post_mlp_norm))
    received_mlp, _, received_out = received_late

    g_mine, small_gathered = _chip_sum(
        [pair_sums[n] for n in big], list(received_in) + list(received_out) + list(received_mlp),
        exchange=_small_gather(*small))
    g_theirs = _run_exchange(_pair_share(g_mine), "grads_pair_share")

    g_big, d_big, nm_big, nv_big, _ = _adamw_big(
        [weights[n] for n in big[1:]], g_mine[1:], g_theirs[1:], [m_in[n] for n in big[1:]], [v_in[n] for n in big[1:]])
    in_t = _adamw_in(w_in.T, g_mine[0], g_theirs[0], m_w_in.T, v_w_in.T)
    g_small, d_small, nm_small, nv_small, loss_total = _small_adamw(
        small_gathered, [row(weights[n]) for n in SMALL_NAMES], [row(m_in[n]) for n in SMALL_NAMES],
        [row(v_in[n]) for n in SMALL_NAMES])

    grads, delta, new_m, new_v = {}, {}, {}, {}
    grads["w_in"], delta["w_in"], new_m["w_in"], new_v["w_in"] = [t.T for t in in_t]
    for i, n in enumerate(big[1:]):
        grads[n], delta[n], new_m[n], new_v[n] = g_big[i], d_big[i], nm_big[i], nv_big[i]
    for i, n in enumerate(SMALL_NAMES):
        shape = weights[n].shape
        grads[n], delta[n], new_m[n], new_v[n] = (g_small[i].reshape(shape), d_small[i].reshape(shape),
                                                  nm_small[i].reshape(shape), nv_small[i].reshape(shape))
    return (loss_total[0, 0], grad_x[None], *[grads[n] for n in order_w], *[delta[n] for n in order_w], *[new_m[n] for n in order_w],
            *[new_v[n] for n in order_w])
```

```python
import jax
import jax.numpy as jnp
from jax import lax
from jax.experimental import pallas as pl
from jax.experimental.pallas import tpu as pltpu

F32 = jnp.float32
BF16 = jnp.bfloat16
MESH = pl.DeviceIdType.MESH

S = 2048
D = 1024
NFH, FHD = 8, 64
NPAIR = NFH // 2
NGH, GHD = 4, 128
DFOX = NFH * FHD
DGDN = NGH * GHD
CHUNK = 64
NCH = S // CHUNK
CONV_K = 4
DFF = 4 * D
EPS = 1e-6
DPROJ = 3600
LANES = 128
DPROJ_PAD = 3840
BLK_GDN = 12
BLK_GZ = 24
BLK_SMALL = 28
NCHIP = 4
NDEV = 8
VMEM_LIMIT = 56 * 1024 * 1024

ADAM_LR = 0.001
ADAM_B1 = 0.9
ADAM_B2 = 0.999
ADAM_EPS = 1e-08
ADAM_WD = 0.01
ADAM_STEP = 10


def _cparams(**kw):
    return pltpu.CompilerParams(vmem_limit_bytes=VMEM_LIMIT, **kw)


def _dn(ca, cb):
    return (((ca,), (cb,)), ((), ()))


def _dot(a, b, ca=1, cb=0):
    return lax.dot_general(a.astype(BF16), b.astype(BF16), _dn(ca, cb), preferred_element_type=F32)


def _hdot(a, b, ca=1, cb=0):
    return lax.dot_general(a.astype(F32), b.astype(F32), _dn(ca, cb), precision=lax.Precision.HIGHEST,
                           preferred_element_type=F32)


def _dot3(a, b, ca=1, cb=0):
    a_hi, b_hi = a.astype(BF16), b.astype(BF16)
    a_lo, b_lo = (a - a_hi.astype(F32)).astype(BF16), (b - b_hi.astype(F32)).astype(BF16)
    dn = _dn(ca, cb)
    return (lax.dot_general(a_hi, b_hi, dn, preferred_element_type=F32)
            + (lax.dot_general(a_hi, b_lo, dn, preferred_element_type=F32)
               + lax.dot_general(a_lo, b_hi, dn, preferred_element_type=F32)))


@jax.custom_vjp
def _mm_nn(a, b):
    return _dot(a, b, 1, 0)


def _mm_nn_fwd(a, b):
    return _dot(a, b, 1, 0), (a, b)


def _mm_nn_bwd(res, g):
    a, b = res
    return _dot(g, b, 1, 1), _dot(a, g, 0, 0)


_mm_nn.defvjp(_mm_nn_fwd, _mm_nn_bwd)


@jax.custom_vjp
def _mm_nt(a, b):
    return _dot(a, b, 1, 1)


def _mm_nt_fwd(a, b):
    return _dot(a, b, 1, 1), (a, b)


def _mm_nt_bwd(res, g):
    a, b = res
    return _dot(g, b, 1, 0), _dot(g, a, 0, 0)


_mm_nt.defvjp(_mm_nt_fwd, _mm_nt_bwd)


@jax.custom_vjp
def _saved_inverse(m, t_inv):
    del m
    return t_inv


def _saved_inverse_fwd(m, t_inv):
    del m
    return t_inv, t_inv


def _saved_inverse_bwd(t_inv, g):
    return -_dot3(_dot3(t_inv, g, 0, 0), t_inv, 1, 1), jnp.zeros_like(t_inv)


_saved_inverse.defvjp(_saved_inverse_fwd, _saved_inverse_bwd)


def _sigmoid(z):
    return 1.0 / (1.0 + jnp.exp(-z))


def _softplus(z):
    return jnp.maximum(z, 0.0) + jnp.log(1.0 + jnp.exp(-jnp.abs(z)))


def _silu(z):
    return z * _sigmoid(z)


def _rms_scale(x):
    return lax.rsqrt(jnp.mean(x * x, axis=-1, keepdims=True) + EPS)


def _rms_bwd(x, w, g):
    r = _rms_scale(x)
    gw = g * w
    dx = r * gw - x * (r * r * r) * jnp.mean(gw * x, axis=-1, keepdims=True)
    return dx, g * x * r


def _matmul(a, b, *, name, ta=False, tb=False, tm=512, tn=512, tk=512, out_dtypes=(F32,), b3=False, o3=False,
            extra=(), epilogue=None, exchange=None):
    m, k = (a.shape[1], a.shape[0]) if ta else a.shape
    if b3:
        n = b.shape[1] if tb else b.shape[0] * b.shape[2]
        kb = b.shape[0] * b.shape[2] if tb else b.shape[1]
    else:
        n, kb = (b.shape[0], b.shape[1]) if tb else (b.shape[1], b.shape[0])
    assert kb == k, (name, kb, k)
    tm, tn, tk = min(tm, m), min(tn, n), min(tk, k)
    assert m % tm == 0 and n % tn == 0 and k % tk == 0, (name, m, n, k, tm, tn, tk)
    nk = k // tk
    whole_k_blocks = b3 and tb and not ta and nk == 1 and b.shape[0] > 1
    n_extra = len(extra)
    n_out = len(out_dtypes)
    grid = (m // tm, n // tn, nk)
    ex_in, ex_in_specs, ex_out_specs, ex_out_shape, ex_scratch = _hosted(exchange)

    def body(*refs):
        a_ref, b_ref = refs[0], refs[1]
        extra_refs = refs[2:2 + n_extra]
        first_out = 2 + n_extra + len(ex_in)
        out_refs = refs[first_out:first_out + n_out]
        ex_refs = refs[2 + n_extra:first_out] + refs[first_out + n_out:first_out + n_out + len(ex_out_shape)] + refs[-2:]
        step = [pl.program_id(d) for d in range(3)]

        if exchange is not None:
            @pl.when((step[0] == 0) & (step[1] == 0) & (step[2] == 0))
            def _():
                exchange.start(*exchange.split(ex_refs))

        def finish(acc):
            outs = (acc,) if epilogue is None else epilogue(acc, *[r[...] for r in extra_refs])
            for o_ref, val in zip(out_refs, outs):
                o_ref[...] = val.astype(o_ref.dtype)

        if whole_k_blocks:
            width = b.shape[2]
            part = _dot(a_ref[:, 0:width], b_ref[0], 1, 1)
            for blk in range(1, b.shape[0]):
                part = part + _dot(a_ref[:, blk * width:(blk + 1) * width], b_ref[blk], 1, 1)
        else:
            part = _dot(a_ref[...], b_ref[...], 0 if ta else 1, 1 if tb else 0)
        if nk == 1:
            finish(part)
        else:
            acc_ref = refs[first_out + n_out + len(ex_out_shape)]

            @pl.when(step[2] == 0)
            def _():
                acc_ref[...] = part

            @pl.when(step[2] > 0)
            def _():
                acc_ref[...] += part

            @pl.when(step[2] == nk - 1)
            def _():
                finish(acc_ref[...])

        if exchange is not None:
            @pl.when((step[0] == grid[0] - 1) & (step[1] == grid[1] - 1) & (step[2] == nk - 1))
            def _():
                exchange.finish(*exchange.split(ex_refs))

    a_spec = pl.BlockSpec((tk, tm), lambda i, j, kk: (kk, i)) if ta else pl.BlockSpec((tm, tk), lambda i, j, kk: (i, kk))
    if whole_k_blocks:
        b_spec = pl.BlockSpec((b.shape[0], tn, b.shape[2]), lambda i, j, kk: (0, j, 0))
    elif b3 and tb:
        assert b.shape[2] == tk
        b_spec = pl.BlockSpec((None, tn, tk), lambda i, j, kk: (kk, j, 0))
    elif b3:
        assert b.shape[2] == tn
        b_spec = pl.BlockSpec((None, tk, tn), lambda i, j, kk: (j, kk, 0))
    elif tb:
        b_spec = pl.BlockSpec((tn, tk), lambda i, j, kk: (j, kk))
    else:
        b_spec = pl.BlockSpec((tk, tn), lambda i, j, kk: (kk, j))
    tile = pl.BlockSpec((tm, tn), lambda i, j, kk: (i, j))
    out_specs = [tile] * n_out
    out_shape = [jax.ShapeDtypeStruct((m, n), dt) for dt in out_dtypes]
    if o3:
        out_specs[0] = pl.BlockSpec((None, tm, tn), lambda i, j, kk: (j, i, 0))
        out_shape[0] = jax.ShapeDtypeStruct((n // tn, m, tn), out_dtypes[0])
    res = pl.pallas_call(
        body, name=name, grid=grid,
        in_specs=[a_spec, b_spec] + [tile] * n_extra + ex_in_specs, out_specs=out_specs + ex_out_specs,
        out_shape=out_shape + ex_out_shape,
        scratch_shapes=([pltpu.VMEM((tm, tn), F32)] if nk > 1 else []) + ex_scratch,
        compiler_params=_cparams(),
    )(a, b, *extra, *ex_in)
    if exchange is not None:
        return (res[0] if n_out == 1 else res[:n_out]), res[n_out:]
    return res[0] if n_out == 1 else res


TR = 256


def _row_spec(cols):
    return pl.BlockSpec((TR, cols), lambda i: (i, 0))


def _vec_spec(cols):
    return pl.BlockSpec((1, cols), lambda i: (0, 0))


def _pre_norm(x, w, exchange=None):
    ex_in, ex_in_specs, ex_out_specs, ex_out_shape, ex_scratch = _hosted(exchange)

    def body(*refs):
        x_ref, w_ref, h_ref = refs[0], refs[1], refs[2 + len(ex_in)]
        ex_refs = refs[2:2 + len(ex_in)] + refs[3 + len(ex_in):]
        if exchange is not None:
            @pl.when(pl.program_id(0) == 0)
            def _():
                exchange.start(*exchange.split(ex_refs))

        xv = x_ref[...]
        h_ref[...] = (xv * _rms_scale(xv) * w_ref[...]).astype(BF16)

        if exchange is not None:
            @pl.when(pl.program_id(0) == S // TR - 1)
            def _():
                exchange.finish(*exchange.split(ex_refs))

    res = pl.pallas_call(
        body, name="pre_norm", grid=(S // TR,), in_specs=[_row_spec(D), _vec_spec(D)] + ex_in_specs,
        out_specs=[_row_spec(D)] + ex_out_specs, out_shape=[jax.ShapeDtypeStruct((S, D), BF16)] + ex_out_shape,
        scratch_shapes=ex_scratch, compiler_params=_cparams(),
    )(x, w, *ex_in)
    return res[0], res[1:]


def _post_mix(x, mixed, w_post, w_pre_mlp):
    def body(x_ref, m_ref, wp_ref, wm_ref, x1_ref, h2_ref):
        mv = m_ref[...]
        x1 = x_ref[...] + mv * _rms_scale(mv) * wp_ref[...]
        x1_ref[...] = x1
        h2_ref[...] = (x1 * _rms_scale(x1) * wm_ref[...]).astype(BF16)

    return pl.pallas_call(
        body, name="post_mix", grid=(S // TR,),
        in_specs=[_row_spec(D), _row_spec(D), _vec_spec(D), _vec_spec(D)], out_specs=[_row_spec(D), _row_spec(D)],
        out_shape=[jax.ShapeDtypeStruct((S, D), F32), jax.ShapeDtypeStruct((S, D), BF16)], compiler_params=_cparams(),
    )(x, mixed, w_post, w_pre_mlp)


def _loss_head(x1, y, w_post_mlp, target):
    def body(x1_ref, y_ref, w_ref, t_ref, dx2_ref, dy_ref, dw_ref, loss_ref):
        i = pl.program_id(0)
        yv = y_ref[...]
        w = w_ref[...]
        x2 = x1_ref[...] + yv * _rms_scale(yv) * w
        err = x2 - t_ref[...]
        dx2 = err * (1.0 / D)
        dx2_ref[...] = dx2
        dy, dwt = _rms_bwd(yv, w, dx2)
        dy_ref[...] = dy.astype(BF16)

        @pl.when(i == 0)
        def _():
            dw_ref[...] = jnp.zeros_like(dw_ref)
            loss_ref[...] = jnp.zeros_like(loss_ref)

        dw_ref[...] += jnp.sum(dwt, axis=0, keepdims=True)
        part = 0.5 * jnp.sum(jnp.mean(err * err, axis=-1, keepdims=True), axis=0, keepdims=True)
        loss_ref[...] += jnp.broadcast_to(part, loss_ref.shape)

    return pl.pallas_call(
        body, name="loss_head", grid=(S // TR,),
        in_specs=[_row_spec(D), _row_spec(D), _vec_spec(D), _row_spec(D)],
        out_specs=[_row_spec(D), _row_spec(D), _vec_spec(D), _vec_spec(LANES)],
        out_shape=[jax.ShapeDtypeStruct((S, D), F32), jax.ShapeDtypeStruct((S, D), BF16),
                   jax.ShapeDtypeStruct((1, D), F32), jax.ShapeDtypeStruct((1, LANES), F32)],
        compiler_params=_cparams(),
    )(x1, y, w_post_mlp, target)


def _mid_bwd(dh2, x1, w_pre_mlp, dx2, mixed, w_post):
    def body(dh2_ref, x1_ref, wm_ref, dx2_ref, m_ref, wp_ref, dx1_ref, dm_ref, dwm_ref, dwp_ref):
        i = pl.program_id(0)
        dxa, dwm = _rms_bwd(x1_ref[...], wm_ref[...], dh2_ref[...])
        dx1 = dx2_ref[...] + dxa
        dx1_ref[...] = dx1
        dm, dwp = _rms_bwd(m_ref[...], wp_ref[...], dx1)
        dm_ref[...] = dm.astype(BF16)

        @pl.when(i == 0)
        def _():
            dwm_ref[...] = jnp.zeros_like(dwm_ref)
            dwp_ref[...] = jnp.zeros_like(dwp_ref)

        dwm_ref[...] += jnp.sum(dwm, axis=0, keepdims=True)
        dwp_ref[...] += jnp.sum(dwp, axis=0, keepdims=True)

    return pl.pallas_call(
        body, name="mid_bwd", grid=(S // TR,),
        in_specs=[_row_spec(D), _row_spec(D), _vec_spec(D), _row_spec(D), _row_spec(D), _vec_spec(D)],
        out_specs=[_row_spec(D), _row_spec(D), _vec_spec(D), _vec_spec(D)],
        out_shape=[jax.ShapeDtypeStruct((S, D), F32), jax.ShapeDtypeStruct((S, D), BF16),
                   jax.ShapeDtypeStruct((1, D), F32), jax.ShapeDtypeStruct((1, D), F32)],
        compiler_params=_cparams(),
    )(dh2, x1, w_pre_mlp, dx2, mixed, w_post)


def _pre_norm_bwd(dh, x, w, dx1):
    def body(dh_ref, x_ref, w_ref, dx1_ref, dx_ref, dw_ref):
        i = pl.program_id(0)
        dxa, dwt = _rms_bwd(x_ref[...], w_ref[...], dh_ref[...])
        dx_ref[...] = dx1_ref[...] + dxa

        @pl.when(i == 0)
        def _():
            dw_ref[...] = jnp.zeros_like(dw_ref)

        dw_ref[...] += jnp.sum(dwt, axis=0, keepdims=True)

    return pl.pallas_call(
        body, name="pre_norm_bwd", grid=(S // TR,),
        in_specs=[_row_spec(D), _row_spec(D), _vec_spec(D), _row_spec(D)], out_specs=[_row_spec(D), _vec_spec(D)],
        out_shape=[jax.ShapeDtypeStruct((S, D), F32), jax.ShapeDtypeStruct((1, D), F32)], compiler_params=_cparams(),
    )(dh, x, w, dx1)


BQ = 512
NQ = S // BQ
LANE_BETA, LANE_G = 8, 12


def _gate_lanes(shape):
    lane = lax.broadcasted_iota(jnp.int32, shape, 1)
    return lane < LANE_BETA, (lane >= LANE_BETA) & (lane < LANE_G), (lane >= LANE_G) & (lane < LANE_G + NGH)


def _gates(proj, bias_vec, alog_vec):
    def body(s_ref, b_ref, a_ref, o_ref, carry_ref):
        i = pl.program_id(0)

        @pl.when(i == 0)
        def _():
            carry_ref[...] = jnp.zeros_like(carry_ref)

        z = s_ref[...] + b_ref[...]
        tail = jnp.log(1.0 + jnp.exp(-jnp.abs(z)))
        sp = jnp.maximum(z, 0.0) + tail
        lf = jnp.minimum(z, 0.0) - tail
        r = lax.broadcasted_iota(jnp.int32, (BQ, BQ), 0)
        c = lax.broadcasted_iota(jnp.int32, (BQ, BQ), 1)
        tri = (c <= r).astype(F32)
        cum = _hdot(tri, lf) + carry_ref[...]
        carry_ref[...] = cum[BQ - 1:BQ, :]
        is_fox, is_beta, is_g = _gate_lanes(z.shape)
        o_ref[...] = jnp.where(is_fox, cum, jnp.where(is_beta, _sigmoid(z), jnp.where(is_g, -jnp.exp(a_ref[...]) * sp, 0.0)))

    return pl.pallas_call(
        body, name="gates", grid=(NQ,),
        in_specs=[pl.BlockSpec((BQ, LANES), lambda i: (i, BLK_SMALL)), _vec_spec(LANES), _vec_spec(LANES)],
        out_specs=pl.BlockSpec((BQ, LANES), lambda i: (i, 0)), out_shape=jax.ShapeDtypeStruct((S, LANES), F32),
        scratch_shapes=[pltpu.VMEM((1, LANES), F32)], compiler_params=_cparams(),
    )(proj, bias_vec, alog_vec)


def _gates_bwd(proj, bias_vec, alog_vec, dgates_gdn, dcum_fox, dproj):
    def body(s_ref, b_ref, a_ref, dg_ref, dc_ref, dproj_in, dproj_ref, red_ref, carry_ref):
        del dproj_in
        i = pl.program_id(0)

        @pl.when(i == 0)
        def _():
            carry_ref[...] = jnp.zeros_like(carry_ref)
            red_ref[...] = jnp.zeros_like(red_ref)

        z = s_ref[...] + b_ref[...]
        dg = dg_ref[...] + dc_ref[...]
        r = lax.broadcasted_iota(jnp.int32, (BQ, BQ), 0)
        c = lax.broadcasted_iota(jnp.int32, (BQ, BQ), 1)
        upper = (c >= r).astype(F32)
        dlf = _hdot(upper, dg) + carry_ref[...]
        carry_ref[...] = dlf[0:1, :]
        sig = _sigmoid(z)
        g_scale = -jnp.exp(a_ref[...])
        is_fox, is_beta, is_g = _gate_lanes(z.shape)
        ds = jnp.where(is_fox, dlf * (1.0 - sig), jnp.where(is_beta, dg * sig * (1.0 - sig), jnp.where(is_g, dg * g_scale * sig, 0.0)))
        dproj_ref[:, 0:LANES] = ds.astype(BF16)
        dproj_ref[:, LANES:2 * LANES] = jnp.zeros((BQ, LANES), BF16)
        dalog = jnp.where(is_g, dg * g_scale * _softplus(z), 0.0)
        sums = jnp.sum(ds, axis=0, keepdims=True)
        red_ref[0:1, :] += jnp.where(is_fox[0:1], sums, 0.0)
        red_ref[1:2, :] += pltpu.roll(jnp.where(is_g[0:1], sums, 0.0), LANES - LANE_G, 1)
        red_ref[2:3, :] += pltpu.roll(jnp.sum(dalog, axis=0, keepdims=True), LANES - LANE_G, 1)

    blk = pl.BlockSpec((BQ, LANES), lambda i: (NQ - 1 - i, 0))
    return pl.pallas_call(
        body, name="gates_bwd", grid=(NQ,),
        in_specs=[pl.BlockSpec((BQ, LANES), lambda i: (NQ - 1 - i, BLK_SMALL)), _vec_spec(LANES), _vec_spec(LANES), blk, blk,
                  pl.BlockSpec(memory_space=pl.ANY)],
        out_specs=[pl.BlockSpec((BQ, 2 * LANES), lambda i: (NQ - 1 - i, BLK_SMALL // 2)), pl.BlockSpec((8, LANES), lambda i: (0, 0))],
        out_shape=[jax.ShapeDtypeStruct((S, DPROJ_PAD), BF16), jax.ShapeDtypeStruct((8, LANES), F32)],
        input_output_aliases={5: 0},
        scratch_shapes=[pltpu.VMEM((1, LANES), F32)], compiler_params=_cparams(),
    )(proj, bias_vec, alog_vec, dgates_gdn, dcum_fox, dproj)


FOX_SCALE = FHD ** -0.5
FOX_PAIRS = 2
FOX_PAIRS_BWD = 2


def _head_mask(e):
    lane = lax.broadcasted_iota(jnp.int32, (1, LANES), 1)
    return (lane >= e * FHD) & (lane < (e + 1) * FHD)


def _lane_col(vals, index):
    lane = lax.broadcasted_iota(jnp.int32, vals.shape, 1)
    return jnp.sum(jnp.where(lane == index, vals, 0.0), axis=1, keepdims=True)


def _sublane_row(vals, index):
    row = lax.broadcasted_iota(jnp.int32, vals.shape, 0)
    return jnp.sum(jnp.where(row == index, vals, 0.0), axis=0, keepdims=True)


def _pair_cols(c0, c1):
    lane = lax.broadcasted_iota(jnp.int32, (c0.shape[0], 2), 1)
    return jnp.where(lane == 0, c0, c1)


def _split3(x):
    hi = x.astype(BF16).astype(F32)
    rest = x - hi
    mid = rest.astype(BF16).astype(F32)
    return hi, mid, (rest - mid).astype(BF16).astype(F32)


def _fox_operand(vals, e, cum, is_query):
    lane = lax.broadcasted_iota(jnp.int32, (1, LANES), 1)
    base = (1 - e) * FHD
    parts = _split3(cum)
    own = jnp.where(_head_mask(e), vals * FOX_SCALE if is_query else vals, 0.0)
    cum_at, ones_at = (base, base + 3) if is_query else (base + 3, base)
    sign = 1.0 if is_query else -1.0
    out = own + jnp.where((lane >= ones_at) & (lane < ones_at + 3), 1.0, 0.0)
    for i, part in enumerate(parts):
        out = out + jnp.where(lane == cum_at + i, sign * part, 0.0)
    return out.astype(BF16)


def _causal_block():
    return lax.broadcasted_iota(jnp.int32, (BQ, BQ), 1) <= lax.broadcasted_iota(jnp.int32, (BQ, BQ), 0)


def _head_rms(o, masks):
    o2 = o * o
    r = [lax.rsqrt(jnp.sum(jnp.where(mk, o2, 0.0), axis=1, keepdims=True) * (1.0 / FHD) + EPS) for mk in masks]
    return jnp.where(masks[0], r[0], r[1])


def _hosted(exchange):
    if exchange is None:
        return [], [], [], [], []
    return (exchange.inputs, [HBM] * len(exchange.inputs), [HBM] * len(exchange.out_shape), exchange.out_shape,
            exchange.sem_shapes())


def _fox_fwd(proj, gates, w2, exchange=None):
    ex_in, ex_in_specs, ex_out_specs, ex_out_shape, ex_scratch = _hosted(exchange)

    n_in = 3 * FOX_PAIRS + 2
    heads = [(pp, e) for pp in range(FOX_PAIRS) for e in range(2)]

    def body(*refs):
        qkv_refs, g_ref, w_ref = refs[:3 * FOX_PAIRS], refs[3 * FOX_PAIRS], refs[3 * FOX_PAIRS + 1]
        mix_ref, o_ref, lse_ref = refs[n_in + len(ex_in):n_in + 3 + len(ex_in)]
        ka_ref, vb_ref = refs[n_in + 3 + len(ex_in) + len(ex_out_shape):n_in + 5 + len(ex_in) + len(ex_out_shape)]
        ex_refs = refs[n_in:n_in + len(ex_in)] + refs[n_in + 3 + len(ex_in):n_in + 3 + len(ex_in) + len(ex_out_shape)] + refs[-2:]
        grp, qi = pl.program_id(0), pl.program_id(1)

        def head_index(pp, e):
            return 2 * (FOX_PAIRS * grp + pp) + e

        if exchange is not None:
            @pl.when((grp == 0) & (qi == 0))
            def _():
                exchange.start(*exchange.split(ex_refs))

        @pl.when(qi == 0)
        def _():
            gt = g_ref[...]
            for pp in range(FOX_PAIRS):
                kv = qkv_refs[3 * pp + 1][...]
                for e in range(2):
                    ka_ref[2 * pp + e] = _fox_operand(kv, e, _lane_col(gt, head_index(pp, e)), False)
                vb_ref[pp] = qkv_refs[3 * pp + 2][...].astype(BF16)

        masks = [_head_mask(0), _head_mask(1)]
        gt = g_ref[pl.ds(pl.multiple_of(qi * BQ, BQ), BQ), :]
        qs = [_fox_operand(qkv_refs[3 * pp][...], e, _lane_col(gt, head_index(pp, e)), True) for pp, e in heads]
        n = range(len(heads))

        def block(kj, carry, diagonal):
            rows = pl.ds(pl.multiple_of(kj * BQ, BQ), BQ)
            s = [_dot(qs[i], ka_ref[i, rows, :], 1, 1) for i in n]
            if diagonal:
                s = [jnp.where(_causal_block(), s[i], -jnp.inf) for i in n]
            m_new = [jnp.maximum(carry[i][0], jnp.max(s[i], axis=-1, keepdims=True)) for i in n]
            p = [jnp.exp(s[i] - m_new[i]) for i in n]
            alpha = [jnp.exp(carry[i][0] - m_new[i]) for i in n]
            l_new = [alpha[i] * carry[i][1] + jnp.sum(p[i], axis=-1, keepdims=True) for i in n]
            pv = [_dot(p[i], vb_ref[heads[i][0], rows, :]) for i in n]
            return tuple((m_new[i], l_new[i], alpha[i] * carry[i][2] + pv[i]) for i in n)

        one = (jnp.full((BQ, 1), -jnp.inf, F32), jnp.zeros((BQ, 1), F32), jnp.zeros((BQ, LANES), F32))
        below = lax.fori_loop(0, qi, lambda kj, carry: block(kj, carry, False), (one,) * len(heads))
        done = block(qi, below, True)
        for pp in range(FOX_PAIRS):
            (m0, l0, a0), (m1, l1, a1) = done[2 * pp], done[2 * pp + 1]
            o = jnp.where(masks[0], a0 / l0, a1 / l1)
            cols = slice(pp * LANES, (pp + 1) * LANES)
            o_ref[:, cols] = o
            mix_ref[:, cols] = (o * _head_rms(o, masks) * w_ref[...]).astype(BF16)
            lse_ref[pp] = _pair_cols(m0 + jnp.log(l0), m1 + jnp.log(l1))

        if exchange is not None:
            @pl.when((grp == NPAIR // FOX_PAIRS - 1) & (qi == NQ - 1))
            def _():
                exchange.finish(*exchange.split(ex_refs))

    qkv_specs = []
    for pp in range(FOX_PAIRS):
        qkv_specs.append(pl.BlockSpec((BQ, LANES), lambda g, i, pp=pp: (i, 3 * (FOX_PAIRS * g + pp))))
        qkv_specs.append(pl.BlockSpec((S, LANES), lambda g, i, pp=pp: (0, 3 * (FOX_PAIRS * g + pp) + 1)))
        qkv_specs.append(pl.BlockSpec((S, LANES), lambda g, i, pp=pp: (0, 3 * (FOX_PAIRS * g + pp) + 2)))
    blk = pl.BlockSpec((BQ, FOX_PAIRS * LANES), lambda g, i: (i, g))
    res = pl.pallas_call(
        body, name="fox_fwd", grid=(NPAIR // FOX_PAIRS, NQ),
        in_specs=qkv_specs + [pl.BlockSpec((S, LANES), lambda g, i: (0, 0)), pl.BlockSpec((1, LANES), lambda g, i: (0, 0))]
        + ex_in_specs,
        out_specs=[blk, blk, pl.BlockSpec((FOX_PAIRS, BQ, 2), lambda g, i: (g, i, 0))] + ex_out_specs,
        out_shape=[jax.ShapeDtypeStruct((S, D), BF16), jax.ShapeDtypeStruct((S, DFOX), F32),
                   jax.ShapeDtypeStruct((NPAIR, S, 2), F32)] + ex_out_shape,
        scratch_shapes=[pltpu.VMEM((2 * FOX_PAIRS, S, LANES), BF16), pltpu.VMEM((FOX_PAIRS, S, LANES), BF16)] + ex_scratch,
        compiler_params=_cparams(),
    )(*([proj] * (3 * FOX_PAIRS)), gates, w2, *ex_in)
    return res[0], res[1], res[2], res[3:]


def _fox_norm_bwd(o, dmix, w2, exchange=None):
    ex_in, ex_in_specs, ex_out_specs, ex_out_shape, ex_scratch = _hosted(exchange)

    def body(*refs):
        o_ref, g_ref, w_ref = refs[:3]
        do_ref, dl_ref, dw_ref = refs[3 + len(ex_in):6 + len(ex_in)]
        ex_refs = refs[3:3 + len(ex_in)] + refs[6 + len(ex_in):]
        hp, qi = pl.program_id(0), pl.program_id(1)

        if exchange is not None:
            @pl.when((hp == 0) & (qi == 0))
            def _():
                exchange.start(*exchange.split(ex_refs))

        masks = [_head_mask(0), _head_mask(1)]
        ov = o_ref[...]
        g = g_ref[...]
        r = _head_rms(ov, masks)
        gw = g * w_ref[...]
        gwo = gw * ov
        mean = [jnp.sum(jnp.where(mk, gwo, 0.0), axis=1, keepdims=True) * (1.0 / FHD) for mk in masks]
        do = r * gw - ov * (r * r * r) * jnp.where(masks[0], mean[0], mean[1])
        do_ref[...] = do.astype(BF16)
        doo = do * ov
        dl_ref[...] = _pair_cols(*[jnp.sum(jnp.where(mk, doo, 0.0), axis=1, keepdims=True) for mk in masks])

        @pl.when((hp == 0) & (qi == 0))
        def _():
            dw_ref[...] = jnp.zeros_like(dw_ref)

        dw_ref[...] += jnp.sum(g * ov * r, axis=0, keepdims=True)

        @pl.when((hp == NPAIR - 1) & (qi == NQ - 1))
        def _():
            dw = dw_ref[...]
            dw_ref[...] = dw + pltpu.roll(dw, FHD, 1)
            if exchange is not None:
                exchange.finish(*exchange.split(ex_refs))

    blk = pl.BlockSpec((BQ, LANES), lambda hp, i: (i, hp))
    vec = pl.BlockSpec((1, LANES), lambda hp, i: (0, 0))
    res = pl.pallas_call(
        body, name="fox_norm_bwd", grid=(NPAIR, NQ), in_specs=[blk, blk, vec] + ex_in_specs,
        out_specs=[blk, pl.BlockSpec((None, BQ, 2), lambda hp, i: (hp, i, 0)), vec] + ex_out_specs,
        out_shape=[jax.ShapeDtypeStruct((S, DFOX), BF16), jax.ShapeDtypeStruct((NPAIR, S, 2), F32),
                   jax.ShapeDtypeStruct((1, LANES), F32)] + ex_out_shape,
        scratch_shapes=ex_scratch, compiler_params=_cparams(),
    )(o, dmix, w2, *ex_in)
    return res[0], res[1], res[2], res[3:]


def _fox_bwd(proj, do, gates, lse, delta, exchange=None):
    ex_in, ex_in_specs, ex_out_specs, ex_out_shape, ex_scratch = _hosted(exchange)

    pg = FOX_PAIRS_BWD
    n_in = 3 * pg + 4
    heads = [(pp, e) for pp in range(pg) for e in range(2)]

    def body(*refs):
        qkv_refs = refs[:3 * pg]
        do_ref, g_ref, lse_ref, dl_ref = refs[3 * pg:n_in]
        dproj_ref, dc_ref = refs[n_in + len(ex_in):n_in + 2 + len(ex_in)]
        qa_ref, dq_ref = refs[n_in + 2 + len(ex_in) + len(ex_out_shape):n_in + 4 + len(ex_in) + len(ex_out_shape)]
        ex_refs = refs[n_in:n_in + len(ex_in)] + refs[n_in + 2 + len(ex_in):n_in + 2 + len(ex_in) + len(ex_out_shape)] + refs[-2:]
        grp, kj = pl.program_id(0), pl.program_id(1)

        def head_index(pp, e):
            return 2 * (pg * grp + pp) + e

        if exchange is not None:
            @pl.when((grp == 0) & (kj == 0))
            def _():
                exchange.start(*exchange.split(ex_refs))

        @pl.when(kj == 0)
        def _():
            gt = g_ref[...]
            for pp in range(pg):
                qv = qkv_refs[3 * pp][...]
                for e in range(2):
                    qa_ref[2 * pp + e] = _fox_operand(qv, e, _lane_col(gt, head_index(pp, e)), True)
            dq_ref[...] = jnp.zeros_like(dq_ref)

        @pl.when((grp == 0) & (kj == 0))
        def _():
            dc_ref[...] = jnp.zeros_like(dc_ref)

        masks = [_head_mask(0), _head_mask(1)]
        krows = pl.ds(pl.multiple_of(kj * BQ, BQ), BQ)
        gk = g_ref[krows, :]
        kas = [_fox_operand(qkv_refs[3 * pp + 1][...], e, _lane_col(gk, head_index(pp, e)), False) for pp, e in heads]
        vbs = [qkv_refs[3 * pp + 2][...].astype(BF16) for pp in range(pg)]
        lane = lax.broadcasted_iota(jnp.int32, (BQ, LANES), 1)
        n = range(len(heads))

        def block(qi, carry, diagonal):
            dks, dvs, css = carry
            rows = pl.ds(pl.multiple_of(qi * BQ, BQ), BQ)
            qa = [qa_ref[i, rows, :] for i in n]
            s = [_dot(qa[i], kas[i], 1, 1) for i in n]
            if diagonal:
                s = [jnp.where(_causal_block(), s[i], -jnp.inf) for i in n]
            dov = [do_ref[rows, pp * LANES:(pp + 1) * LANES] for pp in range(pg)]
            doe = [jnp.where(masks[e], dov[pp], jnp.zeros_like(dov[pp])) for pp, e in heads]
            lse2 = [lse_ref[pp, rows, :] for pp in range(pg)]
            dl2 = [dl_ref[pp, rows, :] for pp in range(pg)]
            p = [jnp.exp(s[i] - _lane_col(lse2[heads[i][0]], heads[i][1])) for i in n]
            dp = [_dot(doe[i], vbs[heads[i][0]], 1, 1) for i in n]
            ds = [p[i] * (dp[i] - _lane_col(dl2[heads[i][0]], heads[i][1])) for i in n]
            dv_part = [_dot(p[i], doe[i], 0, 0) for i in n]
            dk_part = [_dot(ds[i], jnp.where(masks[heads[i][1]], qa[i], jnp.zeros_like(qa[i])), 0, 0) for i in n]
            dq_part = [jnp.where(masks[heads[i][1]], _dot(ds[i], kas[i]), 0.0) for i in n]
            css = tuple(css[i] + jnp.sum(ds[i], axis=0, keepdims=True) for i in n)
            dc = jnp.zeros((BQ, LANES), F32)
            for i in n:
                dc = dc + jnp.where(lane == head_index(*heads[i]), jnp.sum(ds[i], axis=1, keepdims=True), 0.0)
            for pp in range(pg):
                dq_ref[pp, rows, :] += (dq_part[2 * pp] + dq_part[2 * pp + 1]) * FOX_SCALE
            dc_ref[rows, :] += dc
            dks = tuple(dks[pp] + dk_part[2 * pp] + dk_part[2 * pp + 1] for pp in range(pg))
            dvs = tuple(dvs[pp] + dv_part[2 * pp] + dv_part[2 * pp + 1] for pp in range(pg))
            return dks, dvs, css

        zero = jnp.zeros((BQ, LANES), F32)
        first = block(kj, ((zero,) * pg, (zero,) * pg, (jnp.zeros((1, BQ), F32),) * len(heads)), True)
        dks, dvs, css = lax.fori_loop(kj + 1, NQ, lambda qi, carry: block(qi, carry, False), first)
        r = lax.broadcasted_iota(jnp.int32, (BQ, BQ), 0)
        c = lax.broadcasted_iota(jnp.int32, (BQ, BQ), 1)
        dcol = jnp.zeros((BQ, LANES), F32)
        for i in n:
            col = jnp.sum(jnp.where(r == c, css[i], 0.0), axis=1, keepdims=True)
            dcol = dcol + jnp.where(lane == head_index(*heads[i]), col, 0.0)
        dc_ref[krows, :] -= dcol
        for pp in range(pg):
            base = 3 * pp * LANES
            dproj_ref[krows, base + LANES:base + 2 * LANES] = dks[pp].astype(BF16)
            dproj_ref[krows, base + 2 * LANES:base + 3 * LANES] = dvs[pp].astype(BF16)

        @pl.when(kj == NQ - 1)
        def _():
            for pp in range(pg):
                dproj_ref[:, 3 * pp * LANES:(3 * pp + 1) * LANES] = dq_ref[pp].astype(BF16)

        if exchange is not None:
            @pl.when((grp == NPAIR // pg - 1) & (kj == NQ - 1))
            def _():
                exchange.finish(*exchange.split(ex_refs))

    qkv_specs = []
    for pp in range(pg):
        qkv_specs.append(pl.BlockSpec((S, LANES), lambda g, j, pp=pp: (0, 3 * (pg * g + pp))))
        qkv_specs.append(pl.BlockSpec((BQ, LANES), lambda g, j, pp=pp: (j, 3 * (pg * g + pp) + 1)))
        qkv_specs.append(pl.BlockSpec((BQ, LANES), lambda g, j, pp=pp: (j, 3 * (pg * g + pp) + 2)))
    pair = pl.BlockSpec((pg, S, 2), lambda g, j: (g, 0, 0))
    res = pl.pallas_call(
        body, name="fox_bwd", grid=(NPAIR // pg, NQ),
        in_specs=qkv_specs + [pl.BlockSpec((S, pg * LANES), lambda g, j: (0, g)), pl.BlockSpec((S, LANES), lambda g, j: (0, 0)),
                              pair, pair] + ex_in_specs,
        out_specs=[pl.BlockSpec((S, 3 * pg * LANES), lambda g, j: (0, g)), pl.BlockSpec((S, LANES), lambda g, j: (0, 0))]
        + ex_out_specs,
        out_shape=[jax.ShapeDtypeStruct((S, DPROJ_PAD), BF16), jax.ShapeDtypeStruct((S, LANES), F32)] + ex_out_shape,
        scratch_shapes=[pltpu.VMEM((2 * pg, S, LANES), BF16), pltpu.VMEM((pg, S, LANES), F32)] + ex_scratch,
        compiler_params=_cparams(),
    )(*([proj] * (3 * pg)), do, gates, lse, delta, *ex_in)
    return res[0], res[1], res[2:]


NQKV = 3 * NGH
GDN_QSCALE = GHD ** -0.5


def _shift_down(x, s):
    if s == 0:
        return x
    row = lax.broadcasted_iota(jnp.int32, x.shape, 0)
    return jnp.where(row >= s, pltpu.roll(x, s, 0), 0.0)


def _shift_up(x, s):
    if s == 0:
        return x
    n = x.shape[0]
    row = lax.broadcasted_iota(jnp.int32, x.shape, 0)
    return jnp.where(row < n - s, pltpu.roll(x, n - s, 0), 0.0)


def _conv_pre(xv, wv):
    pre = xv * wv[CONV_K - 1:CONV_K, :]
    for j in range(CONV_K - 1):
        pre = pre + _shift_down(xv, CONV_K - 1 - j) * wv[j:j + 1, :]
    return pre


def _l2_factors(b):
    return b < 2 * NGH, jnp.where(b < NGH, GDN_QSCALE, 1.0)


def _gdn_pre(proj, conv_w):
    def body(x_ref, w_ref, o_ref):
        b = pl.program_id(0)
        c = _silu(_conv_pre(x_ref[...], w_ref[...]))
        normed, scale = _l2_factors(b)
        rs = lax.rsqrt(jnp.sum(c * c, axis=-1, keepdims=True) + EPS)
        o_ref[...] = c * jnp.where(normed, rs, 1.0) * scale

    return pl.pallas_call(
        body, name="gdn_pre", grid=(NQKV,),
        in_specs=[pl.BlockSpec((S, GHD), lambda b: (0, BLK_GDN + b)), pl.BlockSpec((CONV_K, GHD), lambda b: (0, b))],
        out_specs=pl.BlockSpec((S, GHD), lambda b: (0, b)),
        out_shape=jax.ShapeDtypeStruct((S, NQKV * GHD), F32), compiler_params=_cparams(),
    )(proj, conv_w)


def _gdn_pre_bwd(proj, conv_w, dqkv, dproj):
    def body(x_ref, w_ref, dy_ref, dproj_in, dx_ref, dw_ref):
        del dproj_in
        b = pl.program_id(0)
        xv = x_ref[...]
        wv = w_ref[...]
        pre = _conv_pre(xv, wv)
        sig = _sigmoid(pre)
        c = pre * sig
        normed, scale = _l2_factors(b)
        g = dy_ref[...] * scale
        rs = lax.rsqrt(jnp.sum(c * c, axis=-1, keepdims=True) + EPS)
        dc_n = rs * g - c * (rs * rs * rs) * jnp.sum(g * c, axis=-1, keepdims=True)
        dc = jnp.where(normed, dc_n, g)
        dpre = dc * sig * (1.0 + pre * (1.0 - sig))
        dx = dpre * wv[CONV_K - 1:CONV_K, :]
        for j in range(CONV_K - 1):
            dx = dx + _shift_up(dpre, CONV_K - 1 - j) * wv[j:j + 1, :]
        dx_ref[...] = dx.astype(BF16)
        for j in range(CONV_K):
            dw_ref[j:j + 1, :] = jnp.sum(dpre * _shift_down(xv, CONV_K - 1 - j), axis=0, keepdims=True)

    return pl.pallas_call(
        body, name="gdn_pre_bwd", grid=(NQKV,),
        in_specs=[pl.BlockSpec((S, GHD), lambda b: (0, BLK_GDN + b)), pl.BlockSpec((CONV_K, GHD), lambda b: (0, b)),
                  pl.BlockSpec((None, S, GHD), lambda b: (b // NGH, 0, b % NGH)), pl.BlockSpec(memory_space=pl.ANY)],
        out_specs=[pl.BlockSpec((S, GHD), lambda b: (0, BLK_GDN + b)), pl.BlockSpec((CONV_K, GHD), lambda b: (0, b))],
        out_shape=[jax.ShapeDtypeStruct((S, DPROJ_PAD), BF16), jax.ShapeDtypeStruct((CONV_K, NQKV * GHD), F32)],
        input_output_aliases={3: 0}, compiler_params=_cparams(),
    )(proj, conv_w, dqkv, dproj)


CB = 16
NCB = NCH // CB


def _chunk_prep(qs, ks, vs, gcols, bcols, t_saved=None):
    n = range(len(qs))
    r = lax.broadcasted_iota(jnp.int32, (CHUNK, CHUNK), 0)
    c = lax.broadcasted_iota(jnp.int32, (CHUNK, CHUNK), 1)
    incl = c <= r
    eye = (r == c).astype(F32)
    grow = [jnp.sum(gcols[i] * eye, axis=0, keepdims=True) for i in n]
    gc_col = [jnp.sum(jnp.where(incl, grow[i], 0.0), axis=1, keepdims=True) for i in n]
    gc_row = [jnp.sum(jnp.where(r <= c, gcols[i], 0.0), axis=0, keepdims=True) for i in n]
    decay = [jnp.exp(jnp.where(incl, gc_col[i] - gc_row[i], -jnp.inf)) for i in n]
    kb = [ks[i] * bcols[i] for i in n]
    vb = [vs[i] * bcols[i] for i in n]
    kk = [_mm_nt(kb[i], ks[i]) for i in n]
    m = [jnp.where(c < r, kk[i] * decay[i], 0.0) for i in n]
    if t_saved is None:
        t_inv = [eye - m[i] for i in n]
        p = [_dot3(m[i], m[i]) for i in n]
        for step in range(5):
            t_inv = [t_inv[i] + _dot3(t_inv[i], p[i]) for i in n]
            if step < 4:
                p = [_dot3(p[i], p[i]) for i in n]
    else:
        t_inv = [_saved_inverse(m[i], t_saved[i]) for i in n]
    egc = [jnp.exp(gc_col[i]) for i in n]
    u = [_mm_nn(t_inv[i], vb[i]) for i in n]
    w = [_mm_nn(t_inv[i], kb[i] * egc[i]) for i in n]
    qk = [_mm_nt(qs[i], ks[i]) for i in n]
    gc_last = [gc_col[i][CHUNK - 1:CHUNK, :] for i in n]
    return [(u[i], w[i], qk[i] * decay[i], qs[i] * egc[i], ks[i] * jnp.exp(gc_last[i] - gc_col[i]), jnp.exp(gc_last[i]),
             t_inv[i]) for i in n]


def _prep_specs():
    rows = CB * CHUNK
    qs = pl.BlockSpec((rows, GHD), lambda i, h: (i, h))
    ks = pl.BlockSpec((rows, GHD), lambda i, h: (i, NGH + h))
    vs = pl.BlockSpec((rows, GHD), lambda i, h: (i, 2 * NGH + h))
    gs = pl.BlockSpec((rows, LANES), lambda i, h: (i, 0))
    a_s = pl.BlockSpec((None, rows, CHUNK), lambda i, h: (h, i, 0))
    gl_s = pl.BlockSpec((None, CB, 1, LANES), lambda i, h: (h, i, 0, 0))
    return qs, ks, vs, gs, a_s, gl_s


def _gdn_prep(qkv, gates, exchange=None):
    ex_in, ex_in_specs, ex_out_specs, ex_out_shape, ex_scratch = _hosted(exchange)

    def body(*refs):
        q_ref, k_ref, v_ref, g_ref = refs[:4]
        u_ref, w_ref, qd_ref, kd_ref, a_ref, gl_ref, t_ref = refs[4 + len(ex_in):11 + len(ex_in)]
        ex_refs = refs[4:4 + len(ex_in)] + refs[11 + len(ex_in):]
        h = pl.program_id(1)

        if exchange is not None:
            @pl.when((pl.program_id(0) == 0) & (h == 0))
            def _():
                exchange.start(*exchange.split(ex_refs))

        chunks = [pl.ds(cidx * CHUNK, CHUNK) for cidx in range(CB)]
        gts = [g_ref[rows, :] for rows in chunks]
        outs = _chunk_prep([q_ref[rows, :] for rows in chunks], [k_ref[rows, :] for rows in chunks],
                           [v_ref[rows, :] for rows in chunks], [_lane_col(gt, LANE_G + h) for gt in gts],
                           [_lane_col(gt, LANE_BETA + h) for gt in gts])
        for cidx, rows in enumerate(chunks):
            u, w, a, qd, kd, gl, t_inv = outs[cidx]
            u_ref[rows, :] = u
            w_ref[rows, :] = w
            qd_ref[rows, :] = qd
            kd_ref[rows, :] = kd
            a_ref[rows, :] = a
            t_ref[rows, :] = t_inv
            gl_ref[cidx] = jnp.broadcast_to(gl, (1, LANES))

        if exchange is not None:
            @pl.when((pl.program_id(0) == NCB - 1) & (h == NGH - 1))
            def _():
                exchange.finish(*exchange.split(ex_refs))

    qs, ks, vs, gs, a_s, gl_s = _prep_specs()
    tok = jax.ShapeDtypeStruct((S, DGDN), F32)
    sq = jax.ShapeDtypeStruct((NGH, S, CHUNK), F32)
    res = pl.pallas_call(
        body, name="gdn_prep", grid=(NCB, NGH), in_specs=[qs, ks, vs, gs] + ex_in_specs,
        out_specs=[qs, qs, qs, qs, a_s, gl_s, a_s] + ex_out_specs,
        out_shape=[tok, tok, tok, tok, sq, jax.ShapeDtypeStruct((NGH, NCH, 1, LANES), F32), sq] + ex_out_shape,
        scratch_shapes=ex_scratch, compiler_params=_cparams(),
    )(qkv, qkv, qkv, gates, *ex_in)
    return res[:7], res[7:]


def _gdn_prep_bwd(qkv, gates, t_inv, du, dw, dqd, dkd, da, dgl, exchange=None):
    ex_in, ex_in_specs, ex_out_specs, ex_out_shape, ex_scratch = _hosted(exchange)

    def body(*refs):
        q_ref, k_ref, v_ref, g_ref, t_ref, du_ref, dw_ref, dqd_ref, dkd_ref, da_ref, dgl_ref = refs[:11]
        dqkv_ref, dg_ref = refs[11 + len(ex_in):13 + len(ex_in)]
        ex_refs = refs[11:11 + len(ex_in)] + refs[13 + len(ex_in):]
        h = pl.program_id(1)

        if exchange is not None:
            @pl.when((pl.program_id(0) == 0) & (h == 0))
            def _():
                exchange.start(*exchange.split(ex_refs))

        @pl.when(h == 0)
        def _():
            dg_ref[...] = jnp.zeros_like(dg_ref)

        lane = lax.broadcasted_iota(jnp.int32, (CHUNK, LANES), 1)
        chunks = [pl.ds(cidx * CHUNK, CHUNK) for cidx in range(CB)]
        gts = [g_ref[rows, :] for rows in chunks]
        t_saved = [t_ref[rows, :] for rows in chunks]
        _, vjp = jax.vjp(lambda *args: [o[:6] for o in _chunk_prep(*args, t_saved=t_saved)],
                         [q_ref[rows, :] for rows in chunks], [k_ref[rows, :] for rows in chunks],
                         [v_ref[rows, :] for rows in chunks], [_lane_col(gt, LANE_G + h) for gt in gts],
                         [_lane_col(gt, LANE_BETA + h) for gt in gts])
        dqs, dks, dvs, dgcs, dbcs = vjp([(du_ref[rows, :], dw_ref[rows, :], da_ref[rows, :], dqd_ref[rows, :],
                                          dkd_ref[rows, :], dgl_ref[cidx][:, 0:1]) for cidx, rows in enumerate(chunks)])
        for cidx, rows in enumerate(chunks):
            dq, dk, dv, dgc, dbc = dqs[cidx], dks[cidx], dvs[cidx], dgcs[cidx], dbcs[cidx]
            dqkv_ref[0, rows, :] = dq
            dqkv_ref[1, rows, :] = dk
            dqkv_ref[2, rows, :] = dv
            dg_ref[rows, :] += jnp.where(lane == LANE_G + h, dgc, 0.0) + jnp.where(lane == LANE_BETA + h, dbc, 0.0)

        if exchange is not None:
            @pl.when((pl.program_id(0) == NCB - 1) & (h == NGH - 1))
            def _():
                exchange.finish(*exchange.split(ex_refs))

    qs, ks, vs, gs, a_s, gl_s = _prep_specs()
    res = pl.pallas_call(
        body, name="gdn_prep_bwd", grid=(NCB, NGH), in_specs=[qs, ks, vs, gs, a_s, qs, qs, qs, qs, a_s, gl_s] + ex_in_specs,
        out_specs=[pl.BlockSpec((3, CB * CHUNK, GHD), lambda i, h: (0, i, h)), gs] + ex_out_specs,
        out_shape=[jax.ShapeDtypeStruct((3, S, DGDN), F32), jax.ShapeDtypeStruct((S, LANES), F32)] + ex_out_shape,
        scratch_shapes=ex_scratch, compiler_params=_cparams(),
    )(qkv, qkv, qkv, gates, t_inv, du, dw, dqd, dkd, da, dgl, *ex_in)
    return res[0], res[1], res[2:]


def _scan_specs(nh, parts, reverse):
    wide, rows, chunks = nh * GHD, S // parts, NCH // parts

    def part(p):
        return parts - 1 - p if reverse else p

    hs = pl.BlockSpec((rows, wide), lambda g, p: (part(p), g))
    a_s = pl.BlockSpec((nh, rows, CHUNK), lambda g, p: (g, part(p), 0))
    gl_s = pl.BlockSpec((nh, chunks, 1, LANES), lambda g, p: (g, part(p), 0, 0))
    st_s = pl.BlockSpec((nh, chunks, GHD, GHD), lambda g, p: (g, part(p), 0, 0))
    gz_s = pl.BlockSpec((rows, wide), lambda g, p: (part(p), BLK_GZ // nh + g))
    mix_s = pl.BlockSpec((rows, wide), lambda g, p: (part(p), NPAIR // nh + g))
    return hs, a_s, gl_s, st_s, gz_s, mix_s


def _head_cols(hh):
    return slice(hh * GHD, (hh + 1) * GHD)


SCAN_HEADS, SCAN_PARTS = 4, 2
SCAN_HEADS_BWD, SCAN_PARTS_BWD = 2, 2


def _gdn_scan(u, w, qd, kd, a, gl, proj, w_norm, mix):
    heads = range(SCAN_HEADS)

    def body(u_ref, w_ref, qd_ref, kd_ref, a_ref, gl_ref, z_ref, wn_ref, mix_in, mix_ref, o_ref, st_ref, carry_ref):
        del mix_in

        @pl.when(pl.program_id(1) == 0)
        def _():
            carry_ref[...] = jnp.zeros_like(carry_ref)

        def step(ci, states):
            rows = pl.ds(pl.multiple_of(ci * CHUNK, CHUNK), CHUNK)
            for hh in heads:
                st_ref[hh, ci] = states[hh]
            ws = [_dot(w_ref[rows, _head_cols(hh)], states[hh]) for hh in heads]
            qs = [_dot(qd_ref[rows, _head_cols(hh)], states[hh]) for hh in heads]
            vn = [u_ref[rows, _head_cols(hh)] - ws[hh] for hh in heads]
            av = [_dot(a_ref[hh, rows, :], vn[hh]) for hh in heads]
            kv = [_dot(kd_ref[rows, _head_cols(hh)], vn[hh], 0, 0) for hh in heads]
            for hh in heads:
                o_ref[rows, _head_cols(hh)] = qs[hh] + av[hh]
            return tuple(states[hh] * gl_ref[hh, ci] + kv[hh] for hh in heads)

        last = lax.fori_loop(0, NCH // SCAN_PARTS, step, tuple(carry_ref[hh] for hh in heads))
        for hh in heads:
            carry_ref[hh] = last[hh]
            ov = o_ref[:, _head_cols(hh)]
            mix_ref[:, _head_cols(hh)] = (ov * _rms_scale(ov) * wn_ref[...] * _silu(z_ref[:, _head_cols(hh)])).astype(BF16)

    hs, a_s, gl_s, st_s, gz_s, mix_s = _scan_specs(SCAN_HEADS, SCAN_PARTS, False)
    return pl.pallas_call(
        body, name="gdn_scan", grid=(NGH // SCAN_HEADS, SCAN_PARTS),
        in_specs=[hs, hs, hs, hs, a_s, gl_s, gz_s, pl.BlockSpec((1, GHD), lambda g, p: (0, 0)),
                  pl.BlockSpec(memory_space=pl.ANY)],
        out_specs=[mix_s, hs, st_s],
        out_shape=[jax.ShapeDtypeStruct((S, D), BF16), jax.ShapeDtypeStruct((S, DGDN), F32),
                   jax.ShapeDtypeStruct((NGH, NCH, GHD, GHD), F32)],
        input_output_aliases={8: 0}, scratch_shapes=[pltpu.VMEM((SCAN_HEADS, GHD, GHD), F32)], compiler_params=_cparams(),
    )(u, w, qd, kd, a, gl, proj, w_norm, mix)


def _gdn_scan_bwd(dmix, o, proj, w_norm, u, w, qd, kd, a, gl, states, dproj, exchange=None):
    ex_in, ex_in_specs, ex_out_specs, ex_out_shape, ex_scratch = _hosted(exchange)
    groups = NGH // SCAN_HEADS_BWD

    def body(*refs):
        dy_ref, o_ref, z_ref, wn_ref, u_ref, w_ref, qd_ref, kd_ref, a_ref, gl_ref, st_ref = refs[:11]
        dz_ref, du_ref, dw_ref, dqd_ref, dkd_ref, da_ref, dgl_ref, dwn_ref = refs[12 + len(ex_in):20 + len(ex_in)]
        do_ref, carry_ref = refs[20 + len(ex_in) + len(ex_out_shape):22 + len(ex_in) + len(ex_out_shape)]
        ex_refs = refs[12:12 + len(ex_in)] + refs[20 + len(ex_in):20 + len(ex_in) + len(ex_out_shape)] + refs[-2:]
        heads = range(SCAN_HEADS_BWD)
        chunks = NCH // SCAN_PARTS_BWD

        if exchange is not None:
            @pl.when((pl.program_id(0) == 0) & (pl.program_id(1) == 0))
            def _():
                exchange.start(*exchange.split(ex_refs))

        @pl.when((pl.program_id(0) == 0) & (pl.program_id(1) == 0))
        def _():
            dwn_ref[...] = jnp.zeros_like(dwn_ref)

        @pl.when(pl.program_id(1) == 0)
        def _():
            carry_ref[...] = jnp.zeros_like(carry_ref)

        wn = wn_ref[...]
        for hh in heads:
            c = _head_cols(hh)
            ov = o_ref[:, c]
            zv = z_ref[:, c]
            g = dy_ref[:, c]
            sig = _sigmoid(zv)
            dz_ref[:, c] = (g * (ov * _rms_scale(ov) * wn) * sig * (1.0 + zv * (1.0 - sig))).astype(BF16)
            do, dwt = _rms_bwd(ov, wn, g * zv * sig)
            do_ref[:, c] = do
            dwn_ref[...] += jnp.sum(dwt, axis=0, keepdims=True)

        def step(t, dstates):
            ci = chunks - 1 - t
            rows = pl.ds(pl.multiple_of(ci * CHUNK, CHUNK), CHUNK)
            cols = [_head_cols(hh) for hh in heads]
            state = [st_ref[hh, ci] for hh in heads]
            dov = [do_ref[rows, cols[hh]] for hh in heads]
            wv = [w_ref[rows, cols[hh]] for hh in heads]
            ws = [_dot(wv[hh], state[hh]) for hh in heads]
            adov = [_dot(a_ref[hh, rows, :], dov[hh], 0, 0) for hh in heads]
            kds = [_dot(kd_ref[rows, cols[hh]], dstates[hh]) for hh in heads]
            dqd = [_dot(dov[hh], state[hh], 1, 1) for hh in heads]
            qdo = [_dot(qd_ref[rows, cols[hh]], dov[hh], 0, 0) for hh in heads]
            vn = [u_ref[rows, cols[hh]] - ws[hh] for hh in heads]
            dvn = [adov[hh] + kds[hh] for hh in heads]
            da = [_dot(dov[hh], vn[hh], 1, 1) for hh in heads]
            dkd = [_dot(vn[hh], dstates[hh], 1, 1) for hh in heads]
            dwv = [_dot(dvn[hh], state[hh], 1, 1) for hh in heads]
            wdv = [_dot(wv[hh], dvn[hh], 0, 0) for hh in heads]
            for hh in heads:
                da_ref[hh, rows, :] = da[hh]
                dqd_ref[rows, cols[hh]] = dqd[hh]
                dkd_ref[rows, cols[hh]] = dkd[hh]
                dgl = jnp.sum(jnp.sum(dstates[hh] * state[hh], axis=1, keepdims=True), axis=0, keepdims=True)
                dgl_ref[hh, ci] = jnp.broadcast_to(dgl, (1, LANES))
                du_ref[rows, cols[hh]] = dvn[hh]
                dw_ref[rows, cols[hh]] = -dwv[hh]
            return tuple(dstates[hh] * gl_ref[hh, ci] + qdo[hh] - wdv[hh] for hh in heads)

        last = lax.fori_loop(0, chunks, step, tuple(carry_ref[hh] for hh in heads))
        for hh in heads:
            carry_ref[hh] = last[hh]

        if exchange is not None:
            @pl.when((pl.program_id(0) == groups - 1) & (pl.program_id(1) == SCAN_PARTS_BWD - 1))
            def _():
                exchange.finish(*exchange.split(ex_refs))

    hs, a_s, gl_s, st_s, gz_s, mix_s = _scan_specs(SCAN_HEADS_BWD, SCAN_PARTS_BWD, True)
    vec = pl.BlockSpec((1, GHD), lambda g, p: (0, 0))
    tok = jax.ShapeDtypeStruct((S, DGDN), F32)
    res = pl.pallas_call(
        body, name="gdn_scan_bwd", grid=(groups, SCAN_PARTS_BWD),
        in_specs=[mix_s, hs, gz_s, vec, hs, hs, hs, hs, a_s, gl_s, st_s, pl.BlockSpec(memory_space=pl.ANY)] + ex_in_specs,
        out_specs=[gz_s, hs, hs, hs, hs, a_s, gl_s, vec] + ex_out_specs,
        out_shape=[jax.ShapeDtypeStruct((S, DPROJ_PAD), BF16), tok, tok, tok, tok,
                   jax.ShapeDtypeStruct((NGH, S, CHUNK), F32), jax.ShapeDtypeStruct((NGH, NCH, 1, LANES), F32),
                   jax.ShapeDtypeStruct((1, GHD), F32)] + ex_out_shape,
        input_output_aliases={11: 0},
        scratch_shapes=[pltpu.VMEM((S // SCAN_PARTS_BWD, SCAN_HEADS_BWD * GHD), F32),
                        pltpu.VMEM((SCAN_HEADS_BWD, GHD, GHD), F32)] + ex_scratch,
        compiler_params=_cparams(),
    )(dmix, o, proj, w_norm, u, w, qd, kd, a, gl, states, dproj, *ex_in)
    return res[:8], res[8:]


def _place():
    return lax.axis_index("x"), lax.axis_index("y"), lax.axis_index("c")


def _other_chips(x, y):
    return [(1 - x, y), (x, 1 - y), (1 - x, 1 - y)]


HBM = pl.BlockSpec(memory_space=pltpu.HBM)
VMEM = pl.BlockSpec(memory_space=pltpu.VMEM)


def _half_rows(ref_or_rows, half):
    rows = ref_or_rows // 2
    return pl.ds(pl.multiple_of(half * rows, rows), rows)


class _Exchange:
    def __init__(self, inputs, out_shape, n_sems, start, finish):
        self.inputs, self.out_shape, self.n_sems, self.start, self.finish = inputs, out_shape, n_sems, start, finish

    def sem_shapes(self):
        return [pltpu.SemaphoreType.DMA((self.n_sems,)), pltpu.SemaphoreType.DMA((self.n_sems,))]

    def split(self, refs):
        n_in, n_out = len(self.inputs), len(self.out_shape)
        return refs[:n_in], refs[n_in:n_in + n_out], refs[n_in + n_out], refs[n_in + n_out + 1]


def _run_exchange(ex, name):
    def body(*refs):
        parts = ex.split(refs)
        ex.start(*parts)
        ex.finish(*parts)

    return pl.pallas_call(
        body, name=name, in_specs=[HBM] * len(ex.inputs), out_specs=[HBM] * len(ex.out_shape), out_shape=ex.out_shape,
        scratch_shapes=ex.sem_shapes(), compiler_params=_cparams(),
    )(*ex.inputs)


def _allgather_exchange(shards, whole=()):
    n, nw = len(shards), len(whole)
    slots = 8

    def plan(src, outs, send_sems, recv_sems):
        x, y, c = _place()
        via_x, via_y, diagonal = _other_chips(x, y)
        id_x, id_y, id_diagonal = [2 * chip[0] + chip[1] for chip in (via_x, via_y, diagonal)]
        me, sibling = (x, y, c), (x, y, 1 - c)

        def rows_of(a, half, quarter):
            total = src[a].shape[0]
            if quarter is None:
                return _half_rows(total, half)
            return pl.ds(pl.multiple_of(half * (total // 2) + quarter * (total // 4), total // 4), total // 4)

        def copy(a, k, chip_index, half, quarter, to, from_src=False):
            rows = rows_of(a, half, quarter)
            dst = outs[a].at[chip_index, rows]
            return pltpu.make_async_remote_copy(
                src_ref=src[a].at[rows] if from_src else dst, dst_ref=dst, send_sem=send_sems.at[slots * a + k],
                recv_sem=recv_sems.at[slots * a + k], device_id=to, device_id_type=MESH)

        def whole_copy(b, k, chip_index, to):
            return pltpu.make_async_remote_copy(
                src_ref=src[n + b], dst_ref=outs[n + b].at[chip_index], send_sem=send_sems.at[slots * n + 3 * b + k],
                recv_sem=recv_sems.at[slots * n + 3 * b + k], device_id=to, device_id_type=MESH)

        first, stages, last = [], [], []
        for a in range(n):
            first += [copy(a, 0, 2 * x + y, c, None, (*via_x, c), True), copy(a, 1, 2 * x + y, c, None, (*via_y, c), True)]
            stages.append([
                (copy(a, 0, id_x, c, None, me),
                 [copy(a, 2, id_x, c, 0, (*via_y, c)), copy(a, 4, id_x, c, None, sibling)]),
                (copy(a, 1, id_y, c, None, me),
                 [copy(a, 3, id_y, c, 1, (*via_x, c)), copy(a, 5, id_y, c, None, sibling)]),
                (copy(a, 2, id_diagonal, c, 0, me), [copy(a, 6, id_diagonal, c, 0, sibling)]),
                (copy(a, 3, id_diagonal, c, 1, me), [copy(a, 7, id_diagonal, c, 1, sibling)]),
            ])
            last += [copy(a, 4, id_x, 1 - c, None, me), copy(a, 5, id_y, 1 - c, None, me),
                     copy(a, 6, id_diagonal, 1 - c, 0, me), copy(a, 7, id_diagonal, 1 - c, 1, me)]
        for b in range(nw):
            for k, (chip, index) in enumerate(((via_x, id_x), (via_y, id_y), (diagonal, id_diagonal))):
                first.append(whole_copy(b, k, 2 * x + y, (*chip, c)))
                last.append(whole_copy(b, k, index, me))
        return first, stages, last

    def start(*refs):
        for cp in plan(*refs)[0]:
            cp.start()

    def finish(*refs):
        first, stages, last = plan(*refs)
        started = list(first)
        for stage in range(4):
            for per_shard in stages:
                lands, onward = per_shard[stage]
                lands.wait_recv()
                for cp in onward:
                    cp.start()
                started += onward
        for cp in last:
            cp.wait_recv()
        for cp in started:
            cp.wait_send()

    out_shape = [jax.ShapeDtypeStruct((NCHIP,) + s.shape, s.dtype) for s in list(shards) + list(whole)]
    return _Exchange(list(shards) + list(whole), out_shape, slots * n + 3 * nw, start, finish)


def _with_own(gathered, own):
    x, y, _ = _place()
    return lax.dynamic_update_index_in_dim(gathered, own, 2 * x + y, axis=0)


def _simple_exchange(inputs, out_shape, copies_of):
    def start(*refs):
        for cp in copies_of(*refs):
            cp.start()

    def finish(*refs):
        for cp in copies_of(*refs):
            cp.wait()

    return _Exchange(list(inputs), out_shape, len(out_shape) * 3, start, finish)


def _pair_exchange(grads):
    def copies_of(src, outs, send_sems, recv_sems):
        x, y, c = _place()
        return [pltpu.make_async_remote_copy(
            src_ref=src[a].at[:, _half_rows(src[a].shape[1], 1 - c)], dst_ref=outs[a], send_sem=send_sems.at[a],
            recv_sem=recv_sems.at[a], device_id=(x, y, 1 - c), device_id_type=MESH) for a in range(len(src))]

    return _simple_exchange(
        grads, [jax.ShapeDtypeStruct((g.shape[0], g.shape[1] // 2, g.shape[2]), g.dtype) for g in grads], copies_of)


def _pair_sum(grads, theirs, name):
    n = len(grads)

    def body(*refs):
        south = lax.axis_index("c") == 0
        for a in range(n):
            g = refs[a][...]
            half = g.shape[0] // 2
            mine = jnp.where(south, g[:half], g[half:])
            refs[2 * n + a][...] = (mine.astype(F32) + refs[n + a][...].astype(F32)).astype(BF16)

    def specs(arrs):
        return [pl.BlockSpec((None,) + g.shape[1:], lambda j: (j, 0, 0)) for g in arrs]

    return pl.pallas_call(
        body, name=name, grid=(NCHIP,), in_specs=specs(grads) + specs(theirs), out_specs=specs(theirs),
        out_shape=[jax.ShapeDtypeStruct(g.shape, BF16) for g in theirs], compiler_params=_cparams(),
    )(*grads, *theirs)


def _chip_exchange(parts):
    def copies_of(src, outs, send_sems, recv_sems):
        x, y, c = _place()
        return [pltpu.make_async_remote_copy(
            src_ref=src[a].at[2 * chip[0] + chip[1]], dst_ref=outs[a].at[k], send_sem=send_sems.at[3 * a + k],
            recv_sem=recv_sems.at[3 * a + k], device_id=(*chip, c), device_id_type=MESH)
            for a in range(len(src)) for k, chip in enumerate(_other_chips(x, y))]

    return _simple_exchange(parts, [jax.ShapeDtypeStruct((NCHIP - 1,) + p.shape[1:], p.dtype) for p in parts], copies_of)


def _chip_sum(parts, received, exchange=None):
    n = len(parts)
    steps = 4
    ex_in, ex_in_specs, ex_out_specs, ex_out_shape, ex_scratch = _hosted(exchange)

    def body(*refs):
        ex_refs = refs[2 * n:2 * n + len(ex_in)] + refs[3 * n + len(ex_in):]
        if exchange is not None:
            @pl.when(pl.program_id(0) == 0)
            def _():
                exchange.start(*exchange.split(ex_refs))

        chip = 2 * lax.axis_index("x") + lax.axis_index("y")
        for a in range(n):
            p, r = refs[a], refs[n + a]
            own = jnp.where(chip == 0, p[0], jnp.where(chip == 1, p[1], jnp.where(chip == 2, p[2], p[3])))
            refs[2 * n + len(ex_in) + a][...] = ((own.astype(F32) + r[0].astype(F32)) + r[1].astype(F32)) + r[2].astype(F32)

        if exchange is not None:
            @pl.when(pl.program_id(0) == steps - 1)
            def _():
                exchange.finish(*exchange.split(ex_refs))

    def specs(arrs):
        return [pl.BlockSpec((g.shape[0], g.shape[1] // steps, g.shape[2]), lambda i: (0, i, 0)) for g in arrs]

    out_specs = [pl.BlockSpec((g.shape[1] // steps, g.shape[2]), lambda i: (i, 0)) for g in parts]
    res = pl.pallas_call(
        body, name="grads_chip_sum", grid=(steps,), in_specs=specs(parts) + specs(received) + ex_in_specs,
        out_specs=out_specs + ex_out_specs, out_shape=[jax.ShapeDtypeStruct(g.shape[1:], F32) for g in parts] + ex_out_shape,
        scratch_shapes=ex_scratch, compiler_params=_cparams(),
    )(*parts, *received, *ex_in)
    return res[:n], res[n:]


def _pair_share(halves):
    def copies_of(src, outs, send_sems, recv_sems):
        x, y, c = _place()
        return [pltpu.make_async_remote_copy(
            src_ref=src[a], dst_ref=outs[a], send_sem=send_sems.at[a], recv_sem=recv_sems.at[a],
            device_id=(x, y, 1 - c), device_id_type=MESH) for a in range(len(src))]

    return _simple_exchange(halves, [jax.ShapeDtypeStruct(h.shape, F32) for h in halves], copies_of)


def _adamw_math(w, g, m, v):
    nm = ADAM_B1 * m + (1.0 - ADAM_B1) * g
    nv = ADAM_B2 * v + (1.0 - ADAM_B2) * jnp.square(g)
    m_hat = nm / (1.0 - ADAM_B1 ** ADAM_STEP)
    v_hat = nv / (1.0 - ADAM_B2 ** ADAM_STEP)
    return -ADAM_LR * (m_hat / (jnp.sqrt(v_hat) + ADAM_EPS) + ADAM_WD * w), nm, nv


def _adamw_big(ws, g_mine, g_theirs, ms, vs, exchange=None):
    n = len(ws)
    steps = 8
    ex_in, ex_in_specs, ex_out_specs, ex_out_shape, ex_scratch = _hosted(exchange)

    def body(*refs):
        ex_refs = refs[5 * n:5 * n + len(ex_in)] + refs[9 * n + len(ex_in):]
        outs = refs[5 * n + len(ex_in):9 * n + len(ex_in)]
        if exchange is not None:
            @pl.when(pl.program_id(0) == 0)
            def _():
                exchange.start(*exchange.split(ex_refs))

        own_half = (pl.program_id(0) // (steps // 2)) == lax.axis_index("c")
        for a in range(n):
            g = jnp.where(own_half, refs[n + a][...], refs[2 * n + a][...])
            d, nm, nv = _adamw_math(refs[a][...], g, refs[3 * n + a][...], refs[4 * n + a][...])
            outs[a][...] = g
            outs[n + a][...] = d
            outs[2 * n + a][...] = nm
            outs[3 * n + a][...] = nv

        if exchange is not None:
            @pl.when(pl.program_id(0) == steps - 1)
            def _():
                exchange.finish(*exchange.split(ex_refs))

    specs = [pl.BlockSpec((w.shape[0] // steps, w.shape[1]), lambda i: (i, 0)) for w in ws]
    half_specs = [pl.BlockSpec((g.shape[0] // (steps // 2), g.shape[1]), lambda i: (i % (steps // 2), 0)) for g in g_mine]
    shapes = [jax.ShapeDtypeStruct(w.shape, F32) for w in ws]
    res = pl.pallas_call(
        body, name="adamw_big", grid=(steps,), in_specs=specs + half_specs * 2 + specs * 2 + ex_in_specs,
        out_specs=specs * 4 + ex_out_specs, out_shape=shapes * 4 + ex_out_shape, scratch_shapes=ex_scratch,
        compiler_params=_cparams(),
    )(*ws, *g_mine, *g_theirs, *ms, *vs, *ex_in)
    return res[:n], res[n:2 * n], res[2 * n:3 * n], res[3 * n:4 * n], res[4 * n:]


def _adamw_in(w, g_mine, g_theirs, m, v):
    half = D // 2

    def body(w_ref, gm_ref, gt_ref, m_ref, v_ref, g_out, d_out, nm_out, nv_out, g_ref):
        south = lax.axis_index("c") == 0
        g_ref[0:half, :] = jnp.where(south, gm_ref[...], gt_ref[...])
        g_ref[half:D, :] = jnp.where(south, gt_ref[...], gm_ref[...])
        g = g_ref[0:CW, :]
        d, nm, nv = _adamw_math(w_ref[...], g, m_ref[...], v_ref[...])
        g_out[...] = g
        d_out[...] = d
        nm_out[...] = nm
        nv_out[...] = nv

    spec = pl.BlockSpec((CW, LANES), lambda i: (0, i))
    half_spec = pl.BlockSpec((half, LANES), lambda i: (0, i))
    return pl.pallas_call(
        body, name="adamw_in", grid=(D // LANES,), in_specs=[spec, half_spec, half_spec, spec, spec], out_specs=[spec] * 4,
        out_shape=[jax.ShapeDtypeStruct((CW, D), F32)] * 4, scratch_shapes=[pltpu.VMEM((D, LANES), F32)],
        compiler_params=_cparams(),
    )(w, g_mine, g_theirs, m, v)


NORM_NAMES = ("pre_mix_norm", "post_mix_norm", "pre_mlp_norm", "post_mlp_norm")
SMALL_NAMES = NORM_NAMES + ("gdn_conv_w", "fox_f_bias", "gdn_dt_bias", "gdn_a_log", "fox_out_norm", "gdn_out_norm")
CONV_COLS = 3 * DGDN // NCHIP


def _small_gather(d_norms, d_conv, sums, d_fox_norm, d_gdn_norm, loss_row):
    n_arrays = 6
    n_remote = n_arrays * (NDEV - 1)

    def copies_of(src, outs, send_sems, recv_sems):
        x, y, c = _place()
        me = 4 * x + 2 * y + c

        def from_me(chip_index):
            cols = pl.ds(pl.multiple_of(chip_index * CONV_COLS, LANES), CONV_COLS)
            return [src[0], src[1].at[:, cols], src[2], src[3], src[4], src[5]]

        local = [pltpu.make_async_copy(s, outs[a].at[me], send_sems.at[n_remote + a]) for a, s in enumerate(from_me(2 * x + y))]
        remote = []
        for k in range(1, NDEV):
            px, py, pc = x ^ ((k >> 2) & 1), y ^ ((k >> 1) & 1), c ^ (k & 1)
            remote += [pltpu.make_async_remote_copy(
                src_ref=s, dst_ref=outs[a].at[me], send_sem=send_sems.at[n_arrays * (k - 1) + a],
                recv_sem=recv_sems.at[n_arrays * (k - 1) + a], device_id=(px, py, pc), device_id_type=MESH)
                for a, s in enumerate(from_me(2 * px + py))]
        return local + remote

    def start(*refs):
        for cp in copies_of(*refs):
            cp.start()

    def finish(*refs):
        for cp in copies_of(*refs):
            cp.wait()

    shapes = [(4, D), (CONV_K, CONV_COLS), (8, LANES), (1, LANES), (1, LANES), (1, LANES)]
    return _Exchange([d_norms, d_conv, sums, d_fox_norm, d_gdn_norm, loss_row],
                     [jax.ShapeDtypeStruct((NDEV,) + s, F32) for s in shapes], n_remote + n_arrays, start, finish)


def _small_adamw(gathered, ws, ms, vs):
    n = len(SMALL_NAMES)
    ng = len(gathered)

    def body(*refs):
        def total(buf):
            acc = buf[0]
            for i in range(1, NDEV):
                acc = acc + buf[i]
            return acc

        t_norms, t_conv, t_sums, t_fn, t_gn, t_loss = [total(r) for r in refs[:ng]]
        w_refs, m_refs, v_refs = refs[ng:ng + n], refs[ng + n:ng + 2 * n], refs[ng + 2 * n:ng + 3 * n]
        outs = refs[ng + 3 * n:]
        outs[4 * n][...] = t_loss
        grads = [t_norms[i:i + 1, :] for i in range(4)] + [
            t_conv, t_sums[0:1, 0:NFH], t_sums[1:2, 0:NGH], t_sums[2:3, 0:NGH], t_fn[:, 0:FHD], t_gn]
        for a in range(n):
            d, nm, nv = _adamw_math(w_refs[a][...], grads[a], m_refs[a][...], v_refs[a][...])
            outs[a][...] = grads[a]
            outs[n + a][...] = d
            outs[2 * n + a][...] = nm
            outs[3 * n + a][...] = nv

    def whole(arr):
        return pl.BlockSpec(arr.shape, lambda i: (0,) * arr.ndim)

    res = pl.pallas_call(
        body, name="small_adamw", grid=(1,), in_specs=[whole(t) for t in gathered] + [whole(w) for w in ws] * 3,
        out_specs=[whole(w) for w in ws] * 4 + [pl.BlockSpec((1, LANES), lambda i: (0, 0))],
        out_shape=[jax.ShapeDtypeStruct(w.shape, F32) for w in ws] * 4 + [jax.ShapeDtypeStruct((1, LANES), F32)],
        compiler_params=_cparams(),
    )(*gathered, *ws, *ms, *vs)
    return res[:n], res[n:2 * n], res[2 * n:3 * n], res[3 * n:4 * n], res[4 * n]


CW = DPROJ // NCHIP
PROJ_RUNS = tuple((part * DFOX + hp * LANES, part * DFOX + (hp + 1) * LANES, (3 * hp + part) * LANES)
                  for hp in range(NPAIR) for part in range(3)) + (
    (1536, 1544, BLK_SMALL * LANES), (1544, 3080, BLK_GDN * LANES), (3080, 3088, BLK_SMALL * LANES + 8),
    (3088, 3600, BLK_GZ * LANES))


def _proj_pieces():
    pieces = []
    for lo, hi, at in PROJ_RUNS:
        while lo < hi:
            j = lo // CW
            end = min(hi, (j + 1) * CW)
            pieces.append((j, lo - j * CW, at, end - lo))
            at, lo = at + end - lo, end
    return pieces


RT = 256


def _to_padded_rows(gathered):
    def body(src_ref, out_ref, blocks_ref, rows_ref):
        blocks_ref[...] = src_ref[...].astype(F32)
        rows_ref[...] = jnp.zeros_like(rows_ref)
        for j, start, at, n in _proj_pieces():
            rows_ref[at:at + n, :] = blocks_ref[j, start:start + n, :]
        out_ref[...] = rows_ref[...].astype(out_ref.dtype)

    return pl.pallas_call(
        body, name="proj_rows_in", grid=(D // RT,), in_specs=[pl.BlockSpec((NCHIP, D, RT), lambda i: (0, 0, i))],
        out_specs=pl.BlockSpec((DPROJ_PAD, RT), lambda i: (0, i)), out_shape=jax.ShapeDtypeStruct((DPROJ_PAD, D), gathered.dtype),
        scratch_shapes=[pltpu.VMEM((NCHIP, D, RT), F32), pltpu.VMEM((DPROJ_PAD, RT), F32)], compiler_params=_cparams(),
    )(gathered)


def _from_padded_rows(w):
    def body(src_ref, out_ref, rows_ref, blocks_ref):
        rows_ref[...] = src_ref[...].astype(F32)
        blocks_ref[...] = jnp.zeros_like(blocks_ref)
        for j, start, at, n in _proj_pieces():
            blocks_ref[j, start:start + n, :] = rows_ref[at:at + n, :]
        out_ref[...] = blocks_ref[...].astype(out_ref.dtype)

    return pl.pallas_call(
        body, name="proj_rows_out", grid=(D // RT,), in_specs=[pl.BlockSpec((DPROJ_PAD, RT), lambda i: (0, i))],
        out_specs=pl.BlockSpec((NCHIP, D, RT), lambda i: (0, 0, i)), out_shape=jax.ShapeDtypeStruct((NCHIP, D, D), w.dtype),
        scratch_shapes=[pltpu.VMEM((DPROJ_PAD, RT), F32), pltpu.VMEM((NCHIP, D, RT), F32)], compiler_params=_cparams(),
    )(w)


def _local_step(x, target, first_weights, late_weights, reduce_late, reduce_in, pre_mix_norm, fox_f_bias, fox_out_norm,
                gdn_a_log, gdn_dt_bias, gdn_out_norm, post_mix_norm, pre_mlp_norm, post_mlp_norm):
    bias_vec = jnp.zeros((1, LANES), F32).at[0, 0:NFH].set(fox_f_bias).at[0, LANE_G:LANE_G + NGH].set(gdn_dt_bias)
    alog_vec = jnp.zeros((1, LANES), F32).at[0, LANE_G:LANE_G + NGH].set(gdn_a_log)
    w2 = jnp.concatenate([fox_out_norm, fox_out_norm], axis=1)

    h, first = _pre_norm(x, pre_mix_norm, exchange=first_weights[0])
    win_p, conv_w = first_weights[1](first)
    proj = _matmul(h, win_p, tb=True, tm=2048, tn=768, tk=1024, name="mm_proj", exchange=late_weights[0])
    proj, late_a = proj if late_weights[0] is not None else (proj, [])
    gates = _gates(proj, bias_vec, alog_vec)
    mix, fox_o, lse, late_b = _fox_fwd(proj, gates, w2, exchange=late_weights[1])
    qkv = _gdn_pre(proj, conv_w)
    (u, w, qd, kd, a_intra, gl, t_inv), late_c = _gdn_prep(qkv, gates, exchange=late_weights[2])
    wout, wup3, wdown = late_weights[3](late_a, late_b, late_c)
    mix, gdn_raw, states = _gdn_scan(u, w, qd, kd, a_intra, gl, proj, gdn_out_norm, mix)
    mixed = _matmul(mix, wout, tm=2048, tk=1024, name="mm_out")
    x1, h2 = _post_mix(x, mixed, post_mix_norm, pre_mlp_norm)

    def relu2(acc):
        r = jnp.maximum(acc, 0.0)
        return r, r * r

    up_relu, act = _matmul(h2, wup3, b3=True, tm=1024, tn=1024, tk=1024, out_dtypes=(BF16, BF16), epilogue=relu2,
                           name="mm_up")
    y = _matmul(act, wdown, tm=1024, tk=DFF, name="mm_down")
    dx2, dy, d_post_mlp, loss_row = _loss_head(x1, y, post_mlp_norm, target)

    dwdown = _matmul(act, dy, ta=True, tm=1024, tn=1024, tk=2048, out_dtypes=(BF16,), name="mm_dwdown")

    def relu2_bwd(acc, r):
        return (acc * 2.0 * r.astype(F32),)

    dup = _matmul(dy, wdown, tb=True, tm=1024, tn=1024, tk=1024, out_dtypes=(BF16,), extra=(up_relu,), epilogue=relu2_bwd,
                  name="mm_dact")
    dwup3 = _matmul(h2, dup, ta=True, tm=1024, tn=1024, tk=2048, out_dtypes=(BF16,), o3=True, name="mm_dwup")
    dh2 = _matmul(dup, wup3, tb=True, b3=True, tm=1024, tk=DFF, name="mm_dh2")
    dx1, dmixed, d_pre_mlp, d_post_mix = _mid_bwd(dh2, x1, pre_mlp_norm, dx2, mixed, post_mix_norm)
    dwout = _matmul(mix, dmixed, ta=True, tm=1024, tn=1024, tk=2048, out_dtypes=(BF16,), name="mm_dwout")
    dmix = _matmul(dmixed, wout, tb=True, tm=2048, tk=1024, name="mm_dmix")

    dfox, delta, d_fox_norm, from_sibling = _fox_norm_bwd(fox_o, dmix, w2, exchange=reduce_late[0](dwout, dwup3, dwdown))
    dproj, dcum_fox, reduced_a = _fox_bwd(proj, dfox, gates, lse, delta, exchange=reduce_late[1](from_sibling))
    (dproj, du, dw, dqd, dkd, da, dgl, d_gdn_norm), reduced_b = _gdn_scan_bwd(
        dmix, gdn_raw, proj, gdn_out_norm, u, w, qd, kd, a_intra, gl, states, dproj, exchange=reduce_late[2]())
    dqkv, dgates_gdn, reduced_c = _gdn_prep_bwd(qkv, gates, t_inv, du, dw, dqd, dkd, da, dgl, exchange=reduce_late[3]())
    reduced_late = (reduced_a, reduced_b, reduced_c)
    dproj, d_conv = _gdn_pre_bwd(proj, conv_w, dqkv, dproj)
    dproj, sums = _gates_bwd(proj, bias_vec, alog_vec, dgates_gdn, dcum_fox, dproj)

    dwin_p = _matmul(dproj, h, ta=True, tm=1280, tn=1024, tk=2048, out_dtypes=(BF16,), name="mm_dwin")
    exchange_in = reduce_in(dwin_p)
    dh = _matmul(dproj, win_p, tm=1024, tk=DPROJ_PAD, name="mm_dh", exchange=exchange_in)
    dh, reduced_in = dh if exchange_in is not None else (dh, [])
    grad_x, d_pre_mix = _pre_norm_bwd(dh, x, pre_mix_norm, dx1)

    d_norms = jnp.concatenate([d_pre_mix, d_post_mix, d_pre_mlp, d_post_mlp], axis=0)
    return grad_x, (d_norms, d_conv, sums, d_fox_norm, d_gdn_norm, loss_row), reduced_late, reduced_in


def kernel(x, pre_mix_norm, w_in, fox_f_bias, fox_out_norm, gdn_conv_w, gdn_a_log, gdn_dt_bias, gdn_out_norm, w_out, post_mix_norm, pre_mlp_norm, w_up, w_down, post_mlp_norm, loss_target, m_pre_mix_norm, m_w_in, m_fox_f_bias, m_fox_out_norm, m_gdn_conv_w, m_gdn_a_log, m_gdn_dt_bias, m_gdn_out_norm, m_w_out, m_post_mix_norm, m_pre_mlp_norm, m_w_up, m_w_down, m_post_mlp_norm, v_pre_mix_norm, v_w_in, v_fox_f_bias, v_fox_out_norm, v_gdn_conv_w, v_gdn_a_log, v_gdn_dt_bias, v_gdn_out_norm, v_w_out, v_post_mix_norm, v_pre_mlp_norm, v_w_up, v_w_down, v_post_mlp_norm):
    weights = dict(pre_mix_norm=pre_mix_norm, w_in=w_in, fox_f_bias=fox_f_bias, fox_out_norm=fox_out_norm, gdn_conv_w=gdn_conv_w,
                   gdn_a_log=gdn_a_log, gdn_dt_bias=gdn_dt_bias, gdn_out_norm=gdn_out_norm, w_out=w_out, post_mix_norm=post_mix_norm,
                   pre_mlp_norm=pre_mlp_norm, w_up=w_up, w_down=w_down, post_mlp_norm=post_mlp_norm)
    m_in = dict(pre_mix_norm=m_pre_mix_norm, w_in=m_w_in, fox_f_bias=m_fox_f_bias, fox_out_norm=m_fox_out_norm, gdn_conv_w=m_gdn_conv_w,
                gdn_a_log=m_gdn_a_log, gdn_dt_bias=m_gdn_dt_bias, gdn_out_norm=m_gdn_out_norm, w_out=m_w_out, post_mix_norm=m_post_mix_norm,
                pre_mlp_norm=m_pre_mlp_norm, w_up=m_w_up, w_down=m_w_down, post_mlp_norm=m_post_mlp_norm)
    v_in = dict(pre_mix_norm=v_pre_mix_norm, w_in=v_w_in, fox_f_bias=v_fox_f_bias, fox_out_norm=v_fox_out_norm, gdn_conv_w=v_gdn_conv_w,
                gdn_a_log=v_gdn_a_log, gdn_dt_bias=v_gdn_dt_bias, gdn_out_norm=v_gdn_out_norm, w_out=v_w_out, post_mix_norm=v_post_mix_norm,
                pre_mlp_norm=v_pre_mlp_norm, w_up=v_w_up, w_down=v_w_down, post_mlp_norm=v_post_mlp_norm)
    order_w = ("pre_mix_norm", "w_in", "fox_f_bias", "fox_out_norm", "gdn_conv_w", "gdn_a_log", "gdn_dt_bias", "gdn_out_norm", "w_out",
               "post_mix_norm", "pre_mlp_norm", "w_up", "w_down", "post_mlp_norm")
    big = ("w_in", "w_out", "w_up", "w_down")

    def row(v):
        return v if v.ndim == 2 else v.reshape(1, -1)

    win_shard = jnp.pad(w_in.T.astype(BF16), ((0, D - CW), (0, 0)))

    def resolve_first(gathered):
        win_g, conv_g = gathered
        return (_to_padded_rows(_with_own(win_g, win_shard)),
                _with_own(conv_g, gdn_conv_w).transpose(1, 0, 2).reshape(CONV_K, 3 * DGDN))

    late_shards = [weights[n].astype(BF16) for n in big[1:]]

    def resolve_late(gathered_out, gathered_mlp, _):
        wout_g, wup3, wdown_g = [_with_own(g, own) for g, own in zip(list(gathered_out) + list(gathered_mlp), late_shards)]
        return wout_g.reshape(D, D), wup3, wdown_g.reshape(DFF, D)

    pair_sums, late_blocks = {}, []

    def pair_summed(names, blocks, theirs):
        for n, s in zip(names, _pair_sum(blocks, theirs, "grads_pair_sum_" + names[0])):
            pair_sums[n] = s

    def late_pair_exchange(dwout, dwup3, dwdown):
        late_blocks.extend([dwout.reshape(NCHIP, D // NCHIP, D), dwup3, dwdown.reshape(NCHIP, DFF // NCHIP, D)])
        return _pair_exchange(late_blocks)

    def late_chip_exchange(theirs):
        pair_summed(big[1:], late_blocks, theirs)
        return _chip_exchange([pair_sums["w_up"], pair_sums["w_down"]])

    def reduce_in(dwin_p):
        blocks = [_from_padded_rows(dwin_p)]
        pair_summed(big[:1], blocks, _run_exchange(_pair_exchange(blocks), "grads_pair_exchange_w_in"))
        return _chip_exchange([pair_sums["w_in"]])

    grad_x, small, received_late, received_in = _local_step(
        x[0], loss_target[0], (_allgather_exchange([win_shard], whole=[gdn_conv_w]), resolve_first),
        (_allgather_exchange(late_shards[:1]), _allgather_exchange(late_shards[1:]), None, resolve_late),
        (late_pair_exchange, late_chip_exchange, lambda: None, lambda: _chip_exchange([pair_sums["w_out"]])),
        reduce_in, row(pre_mix_norm), fox_f_bias, row(fox_out_norm), gdn_a_log, gdn_dt_bias,
        row(gdn_out_norm), row(post_mix_norm), row(pre_mlp_norm), row(post_mlp_norm))
    received_mlp, _, received_out = received_late

    g_mine, small_gathered = _chip_sum(
        [pair_sums[n] for n in big], list(received_in) + list(received_out) + list(received_mlp),
        exchange=_small_gather(*small))
    g_theirs = _run_exchange(_pair_share(g_mine), "grads_pair_share")

    g_big, d_big, nm_big, nv_big, _ = _adamw_big(
        [weights[n] for n in big[1:]], g_mine[1:], g_theirs[1:], [m_in[n] for n in big[1:]], [v_in[n] for n in big[1:]])
    in_t = _adamw_in(w_in.T, g_mine[0], g_theirs[0], m_w_in.T, v_w_in.T)
    g_small, d_small, nm_small, nv_small, loss_total = _small_adamw(
        small_gathered, [row(weights[n]) for n in SMALL_NAMES], [row(m_in[n]) for n in SMALL_NAMES],
        [row(v_in[n]) for n in SMALL_NAMES])

    grads, delta, new_m, new_v = {}, {}, {}, {}
    grads["w_in"], delta["w_in"], new_m["w_in"], new_v["w_in"] = [t.T for t in in_t]
    for i, n in enumerate(big[1:]):
        grads[n], delta[n], new_m[n], new_v[n] = g_big[i], d_big[i], nm_big[i], nv_big[i]
    for i, n in enumerate(SMALL_NAMES):
        shape = weights[n].shape
        grads[n], delta[n], new_m[n], new_v[n] = (g_small[i].reshape(shape), d_small[i].reshape(shape),
                                                  nm_small[i].reshape(shape), nv_small[i].reshape(shape))
    return (loss_total[0, 0], grad_x[None], *[grads[n] for n in order_w], *[delta[n] for n in order_w], *[new_m[n] for n in order_w],
            *[new_v[n] for n in order_w])
```

```python
import jax
import jax.numpy as jnp
from jax import lax
from jax.experimental import pallas as pl
from jax.experimental.pallas import tpu as pltpu

F32 = jnp.float32
BF16 = jnp.bfloat16
MESH = pl.DeviceIdType.MESH

S = 2048
D = 1024
NFH, FHD = 8, 64
NPAIR = NFH // 2
NGH, GHD = 4, 128
DFOX = NFH * FHD
DGDN = NGH * GHD
CHUNK = 64
NCH = S // CHUNK
CONV_K = 4
DFF = 4 * D
EPS = 1e-6
DPROJ = 3600
LANES = 128
DPROJ_PAD = 3840
BLK_GDN = 12
BLK_GZ = 24
BLK_SMALL = 28
NCHIP = 4
NDEV = 8
VMEM_LIMIT = 56 * 1024 * 1024

ADAM_LR = 0.001
ADAM_B1 = 0.9
ADAM_B2 = 0.999
ADAM_EPS = 1e-08
ADAM_WD = 0.01
ADAM_STEP = 10


def _cparams(**kw):
    return pltpu.CompilerParams(vmem_limit_bytes=VMEM_LIMIT, **kw)


def _dn(ca, cb):
    return (((ca,), (cb,)), ((), ()))


def _dot(a, b, ca=1, cb=0):
    return lax.dot_general(a.astype(BF16), b.astype(BF16), _dn(ca, cb), preferred_element_type=F32)


def _hdot(a, b, ca=1, cb=0):
    return lax.dot_general(a.astype(F32), b.astype(F32), _dn(ca, cb), precision=lax.Precision.HIGHEST,
                           preferred_element_type=F32)


def _dot3(a, b, ca=1, cb=0):
    a_hi, b_hi = a.astype(BF16), b.astype(BF16)
    a_lo, b_lo = (a - a_hi.astype(F32)).astype(BF16), (b - b_hi.astype(F32)).astype(BF16)
    dn = _dn(ca, cb)
    return (lax.dot_general(a_hi, b_hi, dn, preferred_element_type=F32)
            + (lax.dot_general(a_hi, b_lo, dn, preferred_element_type=F32)
               + lax.dot_general(a_lo, b_hi, dn, preferred_element_type=F32)))


@jax.custom_vjp
def _mm_nn(a, b):
    return _dot(a, b, 1, 0)


def _mm_nn_fwd(a, b):
    return _dot(a, b, 1, 0), (a, b)


def _mm_nn_bwd(res, g):
    a, b = res
    return _dot(g, b, 1, 1), _dot(a, g, 0, 0)


_mm_nn.defvjp(_mm_nn_fwd, _mm_nn_bwd)


@jax.custom_vjp
def _mm_nt(a, b):
    return _dot(a, b, 1, 1)


def _mm_nt_fwd(a, b):
    return _dot(a, b, 1, 1), (a, b)


def _mm_nt_bwd(res, g):
    a, b = res
    return _dot(g, b, 1, 0), _dot(g, a, 0, 0)


_mm_nt.defvjp(_mm_nt_fwd, _mm_nt_bwd)


@jax.custom_vjp
def _saved_inverse(m, t_inv):
    del m
    return t_inv


def _saved_inverse_fwd(m, t_inv):
    del m
    return t_inv, t_inv


def _saved_inverse_bwd(t_inv, g):
    return -_dot3(_dot3(t_inv, g, 0, 0), t_inv, 1, 1), jnp.zeros_like(t_inv)


_saved_inverse.defvjp(_saved_inverse_fwd, _saved_inverse_bwd)


def _sigmoid(z):
    return 1.0 / (1.0 + jnp.exp(-z))


def _softplus(z):
    return jnp.maximum(z, 0.0) + jnp.log(1.0 + jnp.exp(-jnp.abs(z)))


def _silu(z):
    return z * _sigmoid(z)


def _rms_scale(x):
    return lax.rsqrt(jnp.mean(x * x, axis=-1, keepdims=True) + EPS)


def _rms_bwd(x, w, g):
    r = _rms_scale(x)
    gw = g * w
    dx = r * gw - x * (r * r * r) * jnp.mean(gw * x, axis=-1, keepdims=True)
    return dx, g * x * r


def _matmul(a, b, *, name, ta=False, tb=False, tm=512, tn=512, tk=512, out_dtypes=(F32,), b3=False, o3=False,
            extra=(), epilogue=None, exchange=None):
    m, k = (a.shape[1], a.shape[0]) if ta else a.shape
    if b3:
        n = b.shape[1] if tb else b.shape[0] * b.shape[2]
        kb = b.shape[0] * b.shape[2] if tb else b.shape[1]
    else:
        n, kb = (b.shape[0], b.shape[1]) if tb else (b.shape[1], b.shape[0])
    assert kb == k, (name, kb, k)
    tm, tn, tk = min(tm, m), min(tn, n), min(tk, k)
    assert m % tm == 0 and n % tn == 0 and k % tk == 0, (name, m, n, k, tm, tn, tk)
    nk = k // tk
    whole_k_blocks = b3 and tb and not ta and nk == 1 and b.shape[0] > 1
    n_extra = len(extra)
    n_out = len(out_dtypes)
    grid = (m // tm, n // tn, nk)
    ex_in, ex_in_specs, ex_out_specs, ex_out_shape, ex_scratch = _hosted(exchange)

    def body(*refs):
        a_ref, b_ref = refs[0], refs[1]
        extra_refs = refs[2:2 + n_extra]
        first_out = 2 + n_extra + len(ex_in)
        out_refs = refs[first_out:first_out + n_out]
        ex_refs = refs[2 + n_extra:first_out] + refs[first_out + n_out:first_out + n_out + len(ex_out_shape)] + refs[-2:]
        step = [pl.program_id(d) for d in range(3)]

        if exchange is not None:
            @pl.when((step[0] == 0) & (step[1] == 0) & (step[2] == 0))
            def _():
                exchange.start(*exchange.split(ex_refs))

        def finish(acc):
            outs = (acc,) if epilogue is None else epilogue(acc, *[r[...] for r in extra_refs])
            for o_ref, val in zip(out_refs, outs):
                o_ref[...] = val.astype(o_ref.dtype)

        if whole_k_blocks:
            width = b.shape[2]
            part = _dot(a_ref[:, 0:width], b_ref[0], 1, 1)
            for blk in range(1, b.shape[0]):
                part = part + _dot(a_ref[:, blk * width:(blk + 1) * width], b_ref[blk], 1, 1)
        else:
            part = _dot(a_ref[...], b_ref[...], 0 if ta else 1, 1 if tb else 0)
        if nk == 1:
            finish(part)
        else:
            acc_ref = refs[first_out + n_out + len(ex_out_shape)]

            @pl.when(step[2] == 0)
            def _():
                acc_ref[...] = part

            @pl.when(step[2] > 0)
            def _():
                acc_ref[...] += part

            @pl.when(step[2] == nk - 1)
            def _():
                finish(acc_ref[...])

        if exchange is not None:
            @pl.when((step[0] == grid[0] - 1) & (step[1] == grid[1] - 1) & (step[2] == nk - 1))
            def _():
                exchange.finish(*exchange.split(ex_refs))

    a_spec = pl.BlockSpec((tk, tm), lambda i, j, kk: (kk, i)) if ta else pl.BlockSpec((tm, tk), lambda i, j, kk: (i, kk))
    if whole_k_blocks:
        b_spec = pl.BlockSpec((b.shape[0], tn, b.shape[2]), lambda i, j, kk: (0, j, 0))
    elif b3 and tb:
        assert b.shape[2] == tk
        b_spec = pl.BlockSpec((None, tn, tk), lambda i, j, kk: (kk, j, 0))
    elif b3:
        assert b.shape[2] == tn
        b_spec = pl.BlockSpec((None, tk, tn), lambda i, j, kk: (j, kk, 0))
    elif tb:
        b_spec = pl.BlockSpec((tn, tk), lambda i, j, kk: (j, kk))
    else:
        b_spec = pl.BlockSpec((tk, tn), lambda i, j, kk: (kk, j))
    tile = pl.BlockSpec((tm, tn), lambda i, j, kk: (i, j))
    out_specs = [tile] * n_out
    out_shape = [jax.ShapeDtypeStruct((m, n), dt) for dt in out_dtypes]
    if o3:
        out_specs[0] = pl.BlockSpec((None, tm, tn), lambda i, j, kk: (j, i, 0))
        out_shape[0] = jax.ShapeDtypeStruct((n // tn, m, tn), out_dtypes[0])
    res = pl.pallas_call(
        body, name=name, grid=grid,
        in_specs=[a_spec, b_spec] + [tile] * n_extra + ex_in_specs, out_specs=out_specs + ex_out_specs,
        out_shape=out_shape + ex_out_shape,
        scratch_shapes=([pltpu.VMEM((tm, tn), F32)] if nk > 1 else []) + ex_scratch,
        compiler_params=_cparams(),
    )(a, b, *extra, *ex_in)
    if exchange is not None:
        return (res[0] if n_out == 1 else res[:n_out]), res[n_out:]
    return res[0] if n_out == 1 else res


TR = 256


def _row_spec(cols):
    return pl.BlockSpec((TR, cols), lambda i: (i, 0))


def _vec_spec(cols):
    return pl.BlockSpec((1, cols), lambda i: (0, 0))


def _pre_norm(x, w, exchange=None):
    ex_in, ex_in_specs, ex_out_specs, ex_out_shape, ex_scratch = _hosted(exchange)

    def body(*refs):
        x_ref, w_ref, h_ref = refs[0], refs[1], refs[2 + len(ex_in)]
        ex_refs = refs[2:2 + len(ex_in)] + refs[3 + len(ex_in):]
        if exchange is not None:
            @pl.when(pl.program_id(0) == 0)
            def _():
                exchange.start(*exchange.split(ex_refs))

        xv = x_ref[...]
        h_ref[...] = (xv * _rms_scale(xv) * w_ref[...]).astype(BF16)

        if exchange is not None:
            @pl.when(pl.program_id(0) == S // TR - 1)
            def _():
                exchange.finish(*exchange.split(ex_refs))

    res = pl.pallas_call(
        body, name="pre_norm", grid=(S // TR,), in_specs=[_row_spec(D), _vec_spec(D)] + ex_in_specs,
        out_specs=[_row_spec(D)] + ex_out_specs, out_shape=[jax.ShapeDtypeStruct((S, D), BF16)] + ex_out_shape,
        scratch_shapes=ex_scratch, compiler_params=_cparams(),
    )(x, w, *ex_in)
    return res[0], res[1:]


def _post_mix(x, mixed, w_post, w_pre_mlp):
    def body(x_ref, m_ref, wp_ref, wm_ref, x1_ref, h2_ref):
        mv = m_ref[...]
        x1 = x_ref[...] + mv * _rms_scale(mv) * wp_ref[...]
        x1_ref[...] = x1
        h2_ref[...] = (x1 * _rms_scale(x1) * wm_ref[...]).astype(BF16)

    return pl.pallas_call(
        body, name="post_mix", grid=(S // TR,),
        in_specs=[_row_spec(D), _row_spec(D), _vec_spec(D), _vec_spec(D)], out_specs=[_row_spec(D), _row_spec(D)],
        out_shape=[jax.ShapeDtypeStruct((S, D), F32), jax.ShapeDtypeStruct((S, D), BF16)], compiler_params=_cparams(),
    )(x, mixed, w_post, w_pre_mlp)


def _loss_head(x1, y, w_post_mlp, target):
    def body(x1_ref, y_ref, w_ref, t_ref, dx2_ref, dy_ref, dw_ref, loss_ref):
        i = pl.program_id(0)
        yv = y_ref[...]
        w = w_ref[...]
        x2 = x1_ref[...] + yv * _rms_scale(yv) * w
        err = x2 - t_ref[...]
        dx2 = err * (1.0 / D)
        dx2_ref[...] = dx2
        dy, dwt = _rms_bwd(yv, w, dx2)
        dy_ref[...] = dy.astype(BF16)

        @pl.when(i == 0)
        def _():
            dw_ref[...] = jnp.zeros_like(dw_ref)
            loss_ref[...] = jnp.zeros_like(loss_ref)

        dw_ref[...] += jnp.sum(dwt, axis=0, keepdims=True)
        part = 0.5 * jnp.sum(jnp.mean(err * err, axis=-1, keepdims=True), axis=0, keepdims=True)
        loss_ref[...] += jnp.broadcast_to(part, loss_ref.shape)

    return pl.pallas_call(
        body, name="loss_head", grid=(S // TR,),
        in_specs=[_row_spec(D), _row_spec(D), _vec_spec(D), _row_spec(D)],
        out_specs=[_row_spec(D), _row_spec(D), _vec_spec(D), _vec_spec(LANES)],
        out_shape=[jax.ShapeDtypeStruct((S, D), F32), jax.ShapeDtypeStruct((S, D), BF16),
                   jax.ShapeDtypeStruct((1, D), F32), jax.ShapeDtypeStruct((1, LANES), F32)],
        compiler_params=_cparams(),
    )(x1, y, w_post_mlp, target)


def _mid_bwd(dh2, x1, w_pre_mlp, dx2, mixed, w_post):
    def body(dh2_ref, x1_ref, wm_ref, dx2_ref, m_ref, wp_ref, dx1_ref, dm_ref, dwm_ref, dwp_ref):
        i = pl.program_id(0)
        dxa, dwm = _rms_bwd(x1_ref[...], wm_ref[...], dh2_ref[...])
        dx1 = dx2_ref[...] + dxa
        dx1_ref[...] = dx1
        dm, dwp = _rms_bwd(m_ref[...], wp_ref[...], dx1)
        dm_ref[...] = dm.astype(BF16)

        @pl.when(i == 0)
        def _():
            dwm_ref[...] = jnp.zeros_like(dwm_ref)
            dwp_ref[...] = jnp.zeros_like(dwp_ref)

        dwm_ref[...] += jnp.sum(dwm, axis=0, keepdims=True)
        dwp_ref[...] += jnp.sum(dwp, axis=0, keepdims=True)

    return pl.pallas_call(
        body, name="mid_bwd", grid=(S // TR,),
        in_specs=[_row_spec(D), _row_spec(D), _vec_spec(D), _row_spec(D), _row_spec(D), _vec_spec(D)],
        out_specs=[_row_spec(D), _row_spec(D), _vec_spec(D), _vec_spec(D)],
        out_shape=[jax.ShapeDtypeStruct((S, D), F32), jax.ShapeDtypeStruct((S, D), BF16),
                   jax.ShapeDtypeStruct((1, D), F32), jax.ShapeDtypeStruct((1, D), F32)],
        compiler_params=_cparams(),
    )(dh2, x1, w_pre_mlp, dx2, mixed, w_post)


def _pre_norm_bwd(dh, x, w, dx1):
    def body(dh_ref, x_ref, w_ref, dx1_ref, dx_ref, dw_ref):
        i = pl.program_id(0)
        dxa, dwt = _rms_bwd(x_ref[...], w_ref[...], dh_ref[...])
        dx_ref[...] = dx1_ref[...] + dxa

        @pl.when(i == 0)
        def _():
            dw_ref[...] = jnp.zeros_like(dw_ref)

        dw_ref[...] += jnp.sum(dwt, axis=0, keepdims=True)

    return pl.pallas_call(
        body, name="pre_norm_bwd", grid=(S // TR,),
        in_specs=[_row_spec(D), _row_spec(D), _vec_spec(D), _row_spec(D)], out_specs=[_row_spec(D), _vec_spec(D)],
        out_shape=[jax.ShapeDtypeStruct((S, D), F32), jax.ShapeDtypeStruct((1, D), F32)], compiler_params=_cparams(),
    )(dh, x, w, dx1)


BQ = 512
NQ = S // BQ
LANE_BETA, LANE_G = 8, 12


def _gate_lanes(shape):
    lane = lax.broadcasted_iota(jnp.int32, shape, 1)
    return lane < LANE_BETA, (lane >= LANE_BETA) & (lane < LANE_G), (lane >= LANE_G) & (lane < LANE_G + NGH)


def _gates(proj, bias_vec, alog_vec):
    def body(s_ref, b_ref, a_ref, o_ref, carry_ref):
        i = pl.program_id(0)

        @pl.when(i == 0)
        def _():
            carry_ref[...] = jnp.zeros_like(carry_ref)

        z = s_ref[...] + b_ref[...]
        tail = jnp.log(1.0 + jnp.exp(-jnp.abs(z)))
        sp = jnp.maximum(z, 0.0) + tail
        lf = jnp.minimum(z, 0.0) - tail
        r = lax.broadcasted_iota(jnp.int32, (BQ, BQ), 0)
        c = lax.broadcasted_iota(jnp.int32, (BQ, BQ), 1)
        tri = (c <= r).astype(F32)
        cum = _hdot(tri, lf) + carry_ref[...]
        carry_ref[...] = cum[BQ - 1:BQ, :]
        is_fox, is_beta, is_g = _gate_lanes(z.shape)
        o_ref[...] = jnp.where(is_fox, cum, jnp.where(is_beta, _sigmoid(z), jnp.where(is_g, -jnp.exp(a_ref[...]) * sp, 0.0)))

    return pl.pallas_call(
        body, name="gates", grid=(NQ,),
        in_specs=[pl.BlockSpec((BQ, LANES), lambda i: (i, BLK_SMALL)), _vec_spec(LANES), _vec_spec(LANES)],
        out_specs=pl.BlockSpec((BQ, LANES), lambda i: (i, 0)), out_shape=jax.ShapeDtypeStruct((S, LANES), F32),
        scratch_shapes=[pltpu.VMEM((1, LANES), F32)], compiler_params=_cparams(),
    )(proj, bias_vec, alog_vec)


def _gates_bwd(proj, bias_vec, alog_vec, dgates_gdn, dcum_fox, dproj):
    def body(s_ref, b_ref, a_ref, dg_ref, dc_ref, dproj_in, dproj_ref, red_ref, carry_ref):
        del dproj_in
        i = pl.program_id(0)

        @pl.when(i == 0)
        def _():
            carry_ref[...] = jnp.zeros_like(carry_ref)
            red_ref[...] = jnp.zeros_like(red_ref)

        z = s_ref[...] + b_ref[...]
        dg = dg_ref[...] + dc_ref[...]
        r = lax.broadcasted_iota(jnp.int32, (BQ, BQ), 0)
        c = lax.broadcasted_iota(jnp.int32, (BQ, BQ), 1)
        upper = (c >= r).astype(F32)
        dlf = _hdot(upper, dg) + carry_ref[...]
        carry_ref[...] = dlf[0:1, :]
        sig = _sigmoid(z)
        g_scale = -jnp.exp(a_ref[...])
        is_fox, is_beta, is_g = _gate_lanes(z.shape)
        ds = jnp.where(is_fox, dlf * (1.0 - sig), jnp.where(is_beta, dg * sig * (1.0 - sig), jnp.where(is_g, dg * g_scale * sig, 0.0)))
        dproj_ref[:, 0:LANES] = ds.astype(BF16)
        dproj_ref[:, LANES:2 * LANES] = jnp.zeros((BQ, LANES), BF16)
        dalog = jnp.where(is_g, dg * g_scale * _softplus(z), 0.0)
        sums = jnp.sum(ds, axis=0, keepdims=True)
        red_ref[0:1, :] += jnp.where(is_fox[0:1], sums, 0.0)
        red_ref[1:2, :] += pltpu.roll(jnp.where(is_g[0:1], sums, 0.0), LANES - LANE_G, 1)
        red_ref[2:3, :] += pltpu.roll(jnp.sum(dalog, axis=0, keepdims=True), LANES - LANE_G, 1)

    blk = pl.BlockSpec((BQ, LANES), lambda i: (NQ - 1 - i, 0))
    return pl.pallas_call(
        body, name="gates_bwd", grid=(NQ,),
        in_specs=[pl.BlockSpec((BQ, LANES), lambda i: (NQ - 1 - i, BLK_SMALL)), _vec_spec(LANES), _vec_spec(LANES), blk, blk,
                  pl.BlockSpec(memory_space=pl.ANY)],
        out_specs=[pl.BlockSpec((BQ, 2 * LANES), lambda i: (NQ - 1 - i, BLK_SMALL // 2)), pl.BlockSpec((8, LANES), lambda i: (0, 0))],
        out_shape=[jax.ShapeDtypeStruct((S, DPROJ_PAD), BF16), jax.ShapeDtypeStruct((8, LANES), F32)],
        input_output_aliases={5: 0},
        scratch_shapes=[pltpu.VMEM((1, LANES), F32)], compiler_params=_cparams(),
    )(proj, bias_vec, alog_vec, dgates_gdn, dcum_fox, dproj)


FOX_SCALE = FHD ** -0.5
FOX_PAIRS = 2
FOX_PAIRS_BWD = 2


def _head_mask(e):
    lane = lax.broadcasted_iota(jnp.int32, (1, LANES), 1)
    return (lane >= e * FHD) & (lane < (e + 1) * FHD)


def _lane_col(vals, index):
    lane = lax.broadcasted_iota(jnp.int32, vals.shape, 1)
    return jnp.sum(jnp.where(lane == index, vals, 0.0), axis=1, keepdims=True)


def _sublane_row(vals, index):
    row = lax.broadcasted_iota(jnp.int32, vals.shape, 0)
    return jnp.sum(jnp.where(row == index, vals, 0.0), axis=0, keepdims=True)


def _pair_cols(c0, c1):
    lane = lax.broadcasted_iota(jnp.int32, (c0.shape[0], 2), 1)
    return jnp.where(lane == 0, c0, c1)


def _split3(x):
    hi = x.astype(BF16).astype(F32)
    rest = x - hi
    mid = rest.astype(BF16).astype(F32)
    return hi, mid, (rest - mid).astype(BF16).astype(F32)


def _fox_operand(vals, e, cum, is_query):
    lane = lax.broadcasted_iota(jnp.int32, (1, LANES), 1)
    base = (1 - e) * FHD
    parts = _split3(cum)
    own = jnp.where(_head_mask(e), vals * FOX_SCALE if is_query else vals, 0.0)
    cum_at, ones_at = (base, base + 3) if is_query else (base + 3, base)
    sign = 1.0 if is_query else -1.0
    out = own + jnp.where((lane >= ones_at) & (lane < ones_at + 3), 1.0, 0.0)
    for i, part in enumerate(parts):
        out = out + jnp.where(lane == cum_at + i, sign * part, 0.0)
    return out.astype(BF16)


def _causal_block():
    return lax.broadcasted_iota(jnp.int32, (BQ, BQ), 1) <= lax.broadcasted_iota(jnp.int32, (BQ, BQ), 0)


def _head_rms(o, masks):
    o2 = o * o
    r = [lax.rsqrt(jnp.sum(jnp.where(mk, o2, 0.0), axis=1, keepdims=True) * (1.0 / FHD) + EPS) for mk in masks]
    return jnp.where(masks[0], r[0], r[1])


def _hosted(exchange):
    if exchange is None:
        return [], [], [], [], []
    return (exchange.inputs, [HBM] * len(exchange.inputs), [HBM] * len(exchange.out_shape), exchange.out_shape,
            exchange.sem_shapes())


def _fox_fwd(proj, gates, w2, exchange=None):
    ex_in, ex_in_specs, ex_out_specs, ex_out_shape, ex_scratch = _hosted(exchange)

    n_in = 3 * FOX_PAIRS + 2
    heads = [(pp, e) for pp in range(FOX_PAIRS) for e in range(2)]

    def body(*refs):
        qkv_refs, g_ref, w_ref = refs[:3 * FOX_PAIRS], refs[3 * FOX_PAIRS], refs[3 * FOX_PAIRS + 1]
        mix_ref, o_ref, lse_ref = refs[n_in + len(ex_in):n_in + 3 + len(ex_in)]
        ka_ref, vb_ref = refs[n_in + 3 + len(ex_in) + len(ex_out_shape):n_in + 5 + len(ex_in) + len(ex_out_shape)]
        ex_refs = refs[n_in:n_in + len(ex_in)] + refs[n_in + 3 + len(ex_in):n_in + 3 + len(ex_in) + len(ex_out_shape)] + refs[-2:]
        grp, qi = pl.program_id(0), pl.program_id(1)

        def head_index(pp, e):
            return 2 * (FOX_PAIRS * grp + pp) + e

        if exchange is not None:
            @pl.when((grp == 0) & (qi == 0))
            def _():
                exchange.start(*exchange.split(ex_refs))

        @pl.when(qi == 0)
        def _():
            gt = g_ref[...]
            for pp in range(FOX_PAIRS):
                kv = qkv_refs[3 * pp + 1][...]
                for e in range(2):
                    ka_ref[2 * pp + e] = _fox_operand(kv, e, _lane_col(gt, head_index(pp, e)), False)
                vb_ref[pp] = qkv_refs[3 * pp + 2][...].astype(BF16)

        masks = [_head_mask(0), _head_mask(1)]
        gt = g_ref[pl.ds(pl.multiple_of(qi * BQ, BQ), BQ), :]
        qs = [_fox_operand(qkv_refs[3 * pp][...], e, _lane_col(gt, head_index(pp, e)), True) for pp, e in heads]
        n = range(len(heads))

        def block(kj, carry, diagonal):
            rows = pl.ds(pl.multiple_of(kj * BQ, BQ), BQ)
            s = [_dot(qs[i], ka_ref[i, rows, :], 1, 1) for i in n]
            if diagonal:
                s = [jnp.where(_causal_block(), s[i], -jnp.inf) for i in n]
            m_new = [jnp.maximum(carry[i][0], jnp.max(s[i], axis=-1, keepdims=True)) for i in n]
            p = [jnp.exp(s[i] - m_new[i]) for i in n]
            alpha = [jnp.exp(carry[i][0] - m_new[i]) for i in n]
            l_new = [alpha[i] * carry[i][1] + jnp.sum(p[i], axis=-1, keepdims=True) for i in n]
            pv = [_dot(p[i], vb_ref[heads[i][0], rows, :]) for i in n]
            return tuple((m_new[i], l_new[i], alpha[i] * carry[i][2] + pv[i]) for i in n)

        one = (jnp.full((BQ, 1), -jnp.inf, F32), jnp.zeros((BQ, 1), F32), jnp.zeros((BQ, LANES), F32))
        below = lax.fori_loop(0, qi, lambda kj, carry: block(kj, carry, False), (one,) * len(heads))
        done = block(qi, below, True)
        for pp in range(FOX_PAIRS):
            (m0, l0, a0), (m1, l1, a1) = done[2 * pp], done[2 * pp + 1]
            o = jnp.where(masks[0], a0 / l0, a1 / l1)
            cols = slice(pp * LANES, (pp + 1) * LANES)
            o_ref[:, cols] = o
            mix_ref[:, cols] = (o * _head_rms(o, masks) * w_ref[...]).astype(BF16)
            lse_ref[pp] = _pair_cols(m0 + jnp.log(l0), m1 + jnp.log(l1))

        if exchange is not None:
            @pl.when((grp == NPAIR // FOX_PAIRS - 1) & (qi == NQ - 1))
            def _():
                exchange.finish(*exchange.split(ex_refs))

    qkv_specs = []
    for pp in range(FOX_PAIRS):
        qkv_specs.append(pl.BlockSpec((BQ, LANES), lambda g, i, pp=pp: (i, 3 * (FOX_PAIRS * g + pp))))
        qkv_specs.append(pl.BlockSpec((S, LANES), lambda g, i, pp=pp: (0, 3 * (FOX_PAIRS * g + pp) + 1)))
        qkv_specs.append(pl.BlockSpec((S, LANES), lambda g, i, pp=pp: (0, 3 * (FOX_PAIRS * g + pp) + 2)))
    blk = pl.BlockSpec((BQ, FOX_PAIRS * LANES), lambda g, i: (i, g))
    res = pl.pallas_call(
        body, name="fox_fwd", grid=(NPAIR // FOX_PAIRS, NQ),
        in_specs=qkv_specs + [pl.BlockSpec((S, LANES), lambda g, i: (0, 0)), pl.BlockSpec((1, LANES), lambda g, i: (0, 0))]
        + ex_in_specs,
        out_specs=[blk, blk, pl.BlockSpec((FOX_PAIRS, BQ, 2), lambda g, i: (g, i, 0))] + ex_out_specs,
        out_shape=[jax.ShapeDtypeStruct((S, D), BF16), jax.ShapeDtypeStruct((S, DFOX), F32),
                   jax.ShapeDtypeStruct((NPAIR, S, 2), F32)] + ex_out_shape,
        scratch_shapes=[pltpu.VMEM((2 * FOX_PAIRS, S, LANES), BF16), pltpu.VMEM((FOX_PAIRS, S, LANES), BF16)] + ex_scratch,
        compiler_params=_cparams(),
    )(*([proj] * (3 * FOX_PAIRS)), gates, w2, *ex_in)
    return res[0], res[1], res[2], res[3:]


def _fox_norm_bwd(o, dmix, w2, exchange=None):
    ex_in, ex_in_specs, ex_out_specs, ex_out_shape, ex_scratch = _hosted(exchange)

    def body(*refs):
        o_ref, g_ref, w_ref = refs[:3]
        do_ref, dl_ref, dw_ref = refs[3 + len(ex_in):6 + len(ex_in)]
        ex_refs = refs[3:3 + len(ex_in)] + refs[6 + len(ex_in):]
        hp, qi = pl.program_id(0), pl.program_id(1)

        if exchange is not None:
            @pl.when((hp == 0) & (qi == 0))
            def _():
                exchange.start(*exchange.split(ex_refs))

        masks = [_head_mask(0), _head_mask(1)]
        ov = o_ref[...]
        g = g_ref[...]
        r = _head_rms(ov, masks)
        gw = g * w_ref[...]
        gwo = gw * ov
        mean = [jnp.sum(jnp.where(mk, gwo, 0.0), axis=1, keepdims=True) * (1.0 / FHD) for mk in masks]
        do = r * gw - ov * (r * r * r) * jnp.where(masks[0], mean[0], mean[1])
        do_ref[...] = do.astype(BF16)
        doo = do * ov
        dl_ref[...] = _pair_cols(*[jnp.sum(jnp.where(mk, doo, 0.0), axis=1, keepdims=True) for mk in masks])

        @pl.when((hp == 0) & (qi == 0))
        def _():
            dw_ref[...] = jnp.zeros_like(dw_ref)

        dw_ref[...] += jnp.sum(g * ov * r, axis=0, keepdims=True)

        @pl.when((hp == NPAIR - 1) & (qi == NQ - 1))
        def _():
            dw = dw_ref[...]
            dw_ref[...] = dw + pltpu.roll(dw, FHD, 1)
            if exchange is not None:
                exchange.finish(*exchange.split(ex_refs))

    blk = pl.BlockSpec((BQ, LANES), lambda hp, i: (i, hp))
    vec = pl.BlockSpec((1, LANES), lambda hp, i: (0, 0))
    res = pl.pallas_call(
        body, name="fox_norm_bwd", grid=(NPAIR, NQ), in_specs=[blk, blk, vec] + ex_in_specs,
        out_specs=[blk, pl.BlockSpec((None, BQ, 2), lambda hp, i: (hp, i, 0)), vec] + ex_out_specs,
        out_shape=[jax.ShapeDtypeStruct((S, DFOX), BF16), jax.ShapeDtypeStruct((NPAIR, S, 2), F32),
                   jax.ShapeDtypeStruct((1, LANES), F32)] + ex_out_shape,
        scratch_shapes=ex_scratch, compiler_params=_cparams(),
    )(o, dmix, w2, *ex_in)
    return res[0], res[1], res[2], res[3:]


def _fox_bwd(proj, do, gates, lse, delta, exchange=None):
    ex_in, ex_in_specs, ex_out_specs, ex_out_shape, ex_scratch = _hosted(exchange)

    pg = FOX_PAIRS_BWD
    n_in = 3 * pg + 4
    heads = [(pp, e) for pp in range(pg) for e in range(2)]

    def body(*refs):
        qkv_refs = refs[:3 * pg]
        do_ref, g_ref, lse_ref, dl_ref = refs[3 * pg:n_in]
        dproj_ref, dc_ref = refs[n_in + len(ex_in):n_in + 2 + len(ex_in)]
        qa_ref, dq_ref = refs[n_in + 2 + len(ex_in) + len(ex_out_shape):n_in + 4 + len(ex_in) + len(ex_out_shape)]
        ex_refs = refs[n_in:n_in + len(ex_in)] + refs[n_in + 2 + len(ex_in):n_in + 2 + len(ex_in) + len(ex_out_shape)] + refs[-2:]
        grp, kj = pl.program_id(0), pl.program_id(1)

        def head_index(pp, e):
            return 2 * (pg * grp + pp) + e

        if exchange is not None:
            @pl.when((grp == 0) & (kj == 0))
            def _():
                exchange.start(*exchange.split(ex_refs))

        @pl.when(kj == 0)
        def _():
            gt = g_ref[...]
            for pp in range(pg):
                qv = qkv_refs[3 * pp][...]
                for e in range(2):
                    qa_ref[2 * pp + e] = _fox_operand(qv, e, _lane_col(gt, head_index(pp, e)), True)
            dq_ref[...] = jnp.zeros_like(dq_ref)

        @pl.when((grp == 0) & (kj == 0))
        def _():
            dc_ref[...] = jnp.zeros_like(dc_ref)

        masks = [_head_mask(0), _head_mask(1)]
        krows = pl.ds(pl.multiple_of(kj * BQ, BQ), BQ)
        gk = g_ref[krows, :]
        kas = [_fox_operand(qkv_refs[3 * pp + 1][...], e, _lane_col(gk, head_index(pp, e)), False) for pp, e in heads]
        vbs = [qkv_refs[3 * pp + 2][...].astype(BF16) for pp in range(pg)]
        lane = lax.broadcasted_iota(jnp.int32, (BQ, LANES), 1)
        n = range(len(heads))

        def block(qi, carry, diagonal):
            dks, dvs, css = carry
            rows = pl.ds(pl.multiple_of(qi * BQ, BQ), BQ)
            qa = [qa_ref[i, rows, :] for i in n]
            s = [_dot(qa[i], kas[i], 1, 1) for i in n]
            if diagonal:
                s = [jnp.where(_causal_block(), s[i], -jnp.inf) for i in n]
            dov = [do_ref[rows, pp * LANES:(pp + 1) * LANES] for pp in range(pg)]
            doe = [jnp.where(masks[e], dov[pp], jnp.zeros_like(dov[pp])) for pp, e in heads]
            lse2 = [lse_ref[pp, rows, :] for pp in range(pg)]
            dl2 = [dl_ref[pp, rows, :] for pp in range(pg)]
            p = [jnp.exp(s[i] - _lane_col(lse2[heads[i][0]], heads[i][1])) for i in n]
            dp = [_dot(doe[i], vbs[heads[i][0]], 1, 1) for i in n]
            ds = [p[i] * (dp[i] - _lane_col(dl2[heads[i][0]], heads[i][1])) for i in n]
            dv_part = [_dot(p[i], doe[i], 0, 0) for i in n]
            dk_part = [_dot(ds[i], jnp.where(masks[heads[i][1]], qa[i], jnp.zeros_like(qa[i])), 0, 0) for i in n]
            dq_part = [jnp.where(masks[heads[i][1]], _dot(ds[i], kas[i]), 0.0) for i in n]
            css = tuple(css[i] + jnp.sum(ds[i], axis=0, keepdims=True) for i in n)
            dc = jnp.zeros((BQ, LANES), F32)
            for i in n:
                dc = dc + jnp.where(lane == head_index(*heads[i]), jnp.sum(ds[i], axis=1, keepdims=True), 0.0)
            for pp in range(pg):
                dq_ref[pp, rows, :] += (dq_part[2 * pp] + dq_part[2 * pp + 1]) * FOX_SCALE
            dc_ref[rows, :] += dc
            dks = tuple(dks[pp] + dk_part[2 * pp] + dk_part[2 * pp + 1] for pp in range(pg))
            dvs = tuple(dvs[pp] + dv_part[2 * pp] + dv_part[2 * pp + 1] for pp in range(pg))
            return dks, dvs, css

        zero = jnp.zeros((BQ, LANES), F32)
        first = block(kj, ((zero,) * pg, (zero,) * pg, (jnp.zeros((1, BQ), F32),) * len(heads)), True)
        dks, dvs, css = lax.fori_loop(kj + 1, NQ, lambda qi, carry: block(qi, carry, False), first)
        r = lax.broadcasted_iota(jnp.int32, (BQ, BQ), 0)
        c = lax.broadcasted_iota(jnp.int32, (BQ, BQ), 1)
        dcol = jnp.zeros((BQ, LANES), F32)
        for i in n:
            col = jnp.sum(jnp.where(r == c, css[i], 0.0), axis=1, keepdims=True)
            dcol = dcol + jnp.where(lane == head_index(*heads[i]), col, 0.0)
        dc_ref[krows, :] -= dcol
        for pp in range(pg):
            base = 3 * pp * LANES
            dproj_ref[krows, base + LANES:base + 2 * LANES] = dks[pp].astype(BF16)
            dproj_ref[krows, base + 2 * LANES:base + 3 * LANES] = dvs[pp].astype(BF16)

        @pl.when(kj == NQ - 1)
        def _():
            for pp in range(pg):
                dproj_ref[:, 3 * pp * LANES:(3 * pp + 1) * LANES] = dq_ref[pp].astype(BF16)

        if exchange is not None:
            @pl.when((grp == NPAIR // pg - 1) & (kj == NQ - 1))
            def _():
                exchange.finish(*exchange.split(ex_refs))

    qkv_specs = []
    for pp in range(pg):
        qkv_specs.append(pl.BlockSpec((S, LANES), lambda g, j, pp=pp: (0, 3 * (pg * g + pp))))
        qkv_specs.append(pl.BlockSpec((BQ, LANES), lambda g, j, pp=pp: (j, 3 * (pg * g + pp) + 1)))
        qkv_specs.append(pl.BlockSpec((BQ, LANES), lambda g, j, pp=pp: (j, 3 * (pg * g + pp) + 2)))
    pair = pl.BlockSpec((pg, S, 2), lambda g, j: (g, 0, 0))
    res = pl.pallas_call(
        body, name="fox_bwd", grid=(NPAIR // pg, NQ),
        in_specs=qkv_specs + [pl.BlockSpec((S, pg * LANES), lambda g, j: (0, g)), pl.BlockSpec((S, LANES), lambda g, j: (0, 0)),
                              pair, pair] + ex_in_specs,
        out_specs=[pl.BlockSpec((S, 3 * pg * LANES), lambda g, j: (0, g)), pl.BlockSpec((S, LANES), lambda g, j: (0, 0))]
        + ex_out_specs,
        out_shape=[jax.ShapeDtypeStruct((S, DPROJ_PAD), BF16), jax.ShapeDtypeStruct((S, LANES), F32)] + ex_out_shape,
        scratch_shapes=[pltpu.VMEM((2 * pg, S, LANES), BF16), pltpu.VMEM((pg, S, LANES), F32)] + ex_scratch,
        compiler_params=_cparams(),
    )(*([proj] * (3 * pg)), do, gates, lse, delta, *ex_in)
    return res[0], res[1], res[2:]


NQKV = 3 * NGH
GDN_QSCALE = GHD ** -0.5


def _shift_down(x, s):
    if s == 0:
        return x
    row = lax.broadcasted_iota(jnp.int32, x.shape, 0)
    return jnp.where(row >= s, pltpu.roll(x, s, 0), 0.0)


def _shift_up(x, s):
    if s == 0:
        return x
    n = x.shape[0]
    row = lax.broadcasted_iota(jnp.int32, x.shape, 0)
    return jnp.where(row < n - s, pltpu.roll(x, n - s, 0), 0.0)


def _conv_pre(xv, wv):
    pre = xv * wv[CONV_K - 1:CONV_K, :]
    for j in range(CONV_K - 1):
        pre = pre + _shift_down(xv, CONV_K - 1 - j) * wv[j:j + 1, :]
    return pre


def _l2_factors(b):
    return b < 2 * NGH, jnp.where(b < NGH, GDN_QSCALE, 1.0)


def _gdn_pre(proj, conv_w):
    def body(x_ref, w_ref, o_ref):
        b = pl.program_id(0)
        c = _silu(_conv_pre(x_ref[...], w_ref[...]))
        normed, scale = _l2_factors(b)
        rs = lax.rsqrt(jnp.sum(c * c, axis=-1, keepdims=True) + EPS)
        o_ref[...] = c * jnp.where(normed, rs, 1.0) * scale

    return pl.pallas_call(
        body, name="gdn_pre", grid=(NQKV,),
        in_specs=[pl.BlockSpec((S, GHD), lambda b: (0, BLK_GDN + b)), pl.BlockSpec((CONV_K, GHD), lambda b: (0, b))],
        out_specs=pl.BlockSpec((S, GHD), lambda b: (0, b)),
        out_shape=jax.ShapeDtypeStruct((S, NQKV * GHD), F32), compiler_params=_cparams(),
    )(proj, conv_w)


def _gdn_pre_bwd(proj, conv_w, dqkv, dproj):
    def body(x_ref, w_ref, dy_ref, dproj_in, dx_ref, dw_ref):
        del dproj_in
        b = pl.program_id(0)
        xv = x_ref[...]
        wv = w_ref[...]
        pre = _conv_pre(xv, wv)
        sig = _sigmoid(pre)
        c = pre * sig
        normed, scale = _l2_factors(b)
        g = dy_ref[...] * scale
        rs = lax.rsqrt(jnp.sum(c * c, axis=-1, keepdims=True) + EPS)
        dc_n = rs * g - c * (rs * rs * rs) * jnp.sum(g * c, axis=-1, keepdims=True)
        dc = jnp.where(normed, dc_n, g)
        dpre = dc * sig * (1.0 + pre * (1.0 - sig))
        dx = dpre * wv[CONV_K - 1:CONV_K, :]
        for j in range(CONV_K - 1):
            dx = dx + _shift_up(dpre, CONV_K - 1 - j) * wv[j:j + 1, :]
        dx_ref[...] = dx.astype(BF16)
        for j in range(CONV_K):
            dw_ref[j:j + 1, :] = jnp.sum(dpre * _shift_down(xv, CONV_K - 1 - j), axis=0, keepdims=True)

    return pl.pallas_call(
        body, name="gdn_pre_bwd", grid=(NQKV,),
        in_specs=[pl.BlockSpec((S, GHD), lambda b: (0, BLK_GDN + b)), pl.BlockSpec((CONV_K, GHD), lambda b: (0, b)),
                  pl.BlockSpec((None, S, GHD), lambda b: (b // NGH, 0, b % NGH)), pl.BlockSpec(memory_space=pl.ANY)],
        out_specs=[pl.BlockSpec((S, GHD), lambda b: (0, BLK_GDN + b)), pl.BlockSpec((CONV_K, GHD), lambda b: (0, b))],
        out_shape=[jax.ShapeDtypeStruct((S, DPROJ_PAD), BF16), jax.ShapeDtypeStruct((CONV_K, NQKV * GHD), F32)],
        input_output_aliases={3: 0}, compiler_params=_cparams(),
    )(proj, conv_w, dqkv, dproj)


CB = 16
NCB = NCH // CB


def _chunk_prep(qs, ks, vs, gcols, bcols, t_saved=None):
    n = range(len(qs))
    r = lax.broadcasted_iota(jnp.int32, (CHUNK, CHUNK), 0)
    c = lax.broadcasted_iota(jnp.int32, (CHUNK, CHUNK), 1)
    incl = c <= r
    eye = (r == c).astype(F32)
    grow = [jnp.sum(gcols[i] * eye, axis=0, keepdims=True) for i in n]
    gc_col = [jnp.sum(jnp.where(incl, grow[i], 0.0), axis=1, keepdims=True) for i in n]
    gc_row = [jnp.sum(jnp.where(r <= c, gcols[i], 0.0), axis=0, keepdims=True) for i in n]
    decay = [jnp.exp(jnp.where(incl, gc_col[i] - gc_row[i], -jnp.inf)) for i in n]
    kb = [ks[i] * bcols[i] for i in n]
    vb = [vs[i] * bcols[i] for i in n]
    kk = [_mm_nt(kb[i], ks[i]) for i in n]
    m = [jnp.where(c < r, kk[i] * decay[i], 0.0) for i in n]
    if t_saved is None:
        t_inv = [eye - m[i] for i in n]
        p = [_dot3(m[i], m[i]) for i in n]
        for step in range(5):
            t_inv = [t_inv[i] + _dot3(t_inv[i], p[i]) for i in n]
            if step < 4:
                p = [_dot3(p[i], p[i]) for i in n]
    else:
        t_inv = [_saved_inverse(m[i], t_saved[i]) for i in n]
    egc = [jnp.exp(gc_col[i]) for i in n]
    u = [_mm_nn(t_inv[i], vb[i]) for i in n]
    w = [_mm_nn(t_inv[i], kb[i] * egc[i]) for i in n]
    qk = [_mm_nt(qs[i], ks[i]) for i in n]
    gc_last = [gc_col[i][CHUNK - 1:CHUNK, :] for i in n]
    return [(u[i], w[i], qk[i] * decay[i], qs[i] * egc[i], ks[i] * jnp.exp(gc_last[i] - gc_col[i]), jnp.exp(gc_last[i]),
             t_inv[i]) for i in n]


def _prep_specs():
    rows = CB * CHUNK
    qs = pl.BlockSpec((rows, GHD), lambda i, h: (i, h))
    ks = pl.BlockSpec((rows, GHD), lambda i, h: (i, NGH + h))
    vs = pl.BlockSpec((rows, GHD), lambda i, h: (i, 2 * NGH + h))
    gs = pl.BlockSpec((rows, LANES), lambda i, h: (i, 0))
    a_s = pl.BlockSpec((None, rows, CHUNK), lambda i, h: (h, i, 0))
    gl_s = pl.BlockSpec((None, CB, 1, LANES), lambda i, h: (h, i, 0, 0))
    return qs, ks, vs, gs, a_s, gl_s


def _gdn_prep(qkv, gates, exchange=None):
    ex_in, ex_in_specs, ex_out_specs, ex_out_shape, ex_scratch = _hosted(exchange)

    def body(*refs):
        q_ref, k_ref, v_ref, g_ref = refs[:4]
        u_ref, w_ref, qd_ref, kd_ref, a_ref, gl_ref, t_ref = refs[4 + len(ex_in):11 + len(ex_in)]
        ex_refs = refs[4:4 + len(ex_in)] + refs[11 + len(ex_in):]
        h = pl.program_id(1)

        if exchange is not None:
            @pl.when((pl.program_id(0) == 0) & (h == 0))
            def _():
                exchange.start(*exchange.split(ex_refs))

        chunks = [pl.ds(cidx * CHUNK, CHUNK) for cidx in range(CB)]
        gts = [g_ref[rows, :] for rows in chunks]
        outs = _chunk_prep([q_ref[rows, :] for rows in chunks], [k_ref[rows, :] for rows in chunks],
                           [v_ref[rows, :] for rows in chunks], [_lane_col(gt, LANE_G + h) for gt in gts],
                           [_lane_col(gt, LANE_BETA + h) for gt in gts])
        for cidx, rows in enumerate(chunks):
            u, w, a, qd, kd, gl, t_inv = outs[cidx]
            u_ref[rows, :] = u
            w_ref[rows, :] = w
            qd_ref[rows, :] = qd
            kd_ref[rows, :] = kd
            a_ref[rows, :] = a
            t_ref[rows, :] = t_inv
            gl_ref[cidx] = jnp.broadcast_to(gl, (1, LANES))

        if exchange is not None:
            @pl.when((pl.program_id(0) == NCB - 1) & (h == NGH - 1))
            def _():
                exchange.finish(*exchange.split(ex_refs))

    qs, ks, vs, gs, a_s, gl_s = _prep_specs()
    tok = jax.ShapeDtypeStruct((S, DGDN), F32)
    sq = jax.ShapeDtypeStruct((NGH, S, CHUNK), F32)
    res = pl.pallas_call(
        body, name="gdn_prep", grid=(NCB, NGH), in_specs=[qs, ks, vs, gs] + ex_in_specs,
        out_specs=[qs, qs, qs, qs, a_s, gl_s, a_s] + ex_out_specs,
        out_shape=[tok, tok, tok, tok, sq, jax.ShapeDtypeStruct((NGH, NCH, 1, LANES), F32), sq] + ex_out_shape,
        scratch_shapes=ex_scratch, compiler_params=_cparams(),
    )(qkv, qkv, qkv, gates, *ex_in)
    return res[:7], res[7:]


def _gdn_prep_bwd(qkv, gates, t_inv, du, dw, dqd, dkd, da, dgl, exchange=None):
    ex_in, ex_in_specs, ex_out_specs, ex_out_shape, ex_scratch = _hosted(exchange)

    def body(*refs):
        q_ref, k_ref, v_ref, g_ref, t_ref, du_ref, dw_ref, dqd_ref, dkd_ref, da_ref, dgl_ref = refs[:11]
        dqkv_ref, dg_ref = refs[11 + len(ex_in):13 + len(ex_in)]
        ex_refs = refs[11:11 + len(ex_in)] + refs[13 + len(ex_in):]
        h = pl.program_id(1)

        if exchange is not None:
            @pl.when((pl.program_id(0) == 0) & (h == 0))
            def _():
                exchange.start(*exchange.split(ex_refs))

        @pl.when(h == 0)
        def _():
            dg_ref[...] = jnp.zeros_like(dg_ref)

        lane = lax.broadcasted_iota(jnp.int32, (CHUNK, LANES), 1)
        chunks = [pl.ds(cidx * CHUNK, CHUNK) for cidx in range(CB)]
        gts = [g_ref[rows, :] for rows in chunks]
        t_saved = [t_ref[rows, :] for rows in chunks]
        _, vjp = jax.vjp(lambda *args: [o[:6] for o in _chunk_prep(*args, t_saved=t_saved)],
                         [q_ref[rows, :] for rows in chunks], [k_ref[rows, :] for rows in chunks],
                         [v_ref[rows, :] for rows in chunks], [_lane_col(gt, LANE_G + h) for gt in gts],
                         [_lane_col(gt, LANE_BETA + h) for gt in gts])
        dqs, dks, dvs, dgcs, dbcs = vjp([(du_ref[rows, :], dw_ref[rows, :], da_ref[rows, :], dqd_ref[rows, :],
                                          dkd_ref[rows, :], dgl_ref[cidx][:, 0:1]) for cidx, rows in enumerate(chunks)])
        for cidx, rows in enumerate(chunks):
            dq, dk, dv, dgc, dbc = dqs[cidx], dks[cidx], dvs[cidx], dgcs[cidx], dbcs[cidx]
            dqkv_ref[0, rows, :] = dq
            dqkv_ref[1, rows, :] = dk
            dqkv_ref[2, rows, :] = dv
            dg_ref[rows, :] += jnp.where(lane == LANE_G + h, dgc, 0.0) + jnp.where(lane == LANE_BETA + h, dbc, 0.0)

        if exchange is not None:
            @pl.when((pl.program_id(0) == NCB - 1) & (h == NGH - 1))
            def _():
                exchange.finish(*exchange.split(ex_refs))

    qs, ks, vs, gs, a_s, gl_s = _prep_specs()
    res = pl.pallas_call(
        body, name="gdn_prep_bwd", grid=(NCB, NGH), in_specs=[qs, ks, vs, gs, a_s, qs, qs, qs, qs, a_s, gl_s] + ex_in_specs,
        out_specs=[pl.BlockSpec((3, CB * CHUNK, GHD), lambda i, h: (0, i, h)), gs] + ex_out_specs,
        out_shape=[jax.ShapeDtypeStruct((3, S, DGDN), F32), jax.ShapeDtypeStruct((S, LANES), F32)] + ex_out_shape,
        scratch_shapes=ex_scratch, compiler_params=_cparams(),
    )(qkv, qkv, qkv, gates, t_inv, du, dw, dqd, dkd, da, dgl, *ex_in)
    return res[0], res[1], res[2:]


def _scan_specs(nh, parts, reverse):
    wide, rows, chunks = nh * GHD, S // parts, NCH // parts

    def part(p):
        return parts - 1 - p if reverse else p

    hs = pl.BlockSpec((rows, wide), lambda g, p: (part(p), g))
    a_s = pl.BlockSpec((nh, rows, CHUNK), lambda g, p: (g, part(p), 0))
    gl_s = pl.BlockSpec((nh, chunks, 1, LANES), lambda g, p: (g, part(p), 0, 0))
    st_s = pl.BlockSpec((nh, chunks, GHD, GHD), lambda g, p: (g, part(p), 0, 0))
    gz_s = pl.BlockSpec((rows, wide), lambda g, p: (part(p), BLK_GZ // nh + g))
    mix_s = pl.BlockSpec((rows, wide), lambda g, p: (part(p), NPAIR // nh + g))
    return hs, a_s, gl_s, st_s, gz_s, mix_s


def _head_cols(hh):
    return slice(hh * GHD, (hh + 1) * GHD)


SCAN_HEADS, SCAN_PARTS = 4, 2
SCAN_HEADS_BWD, SCAN_PARTS_BWD = 2, 2


def _gdn_scan(u, w, qd, kd, a, gl, proj, w_norm, mix):
    heads = range(SCAN_HEADS)

    def body(u_ref, w_ref, qd_ref, kd_ref, a_ref, gl_ref, z_ref, wn_ref, mix_in, mix_ref, o_ref, st_ref, carry_ref):
        del mix_in

        @pl.when(pl.program_id(1) == 0)
        def _():
            carry_ref[...] = jnp.zeros_like(carry_ref)

        def step(ci, states):
            rows = pl.ds(pl.multiple_of(ci * CHUNK, CHUNK), CHUNK)
            for hh in heads:
                st_ref[hh, ci] = states[hh]
            ws = [_dot(w_ref[rows, _head_cols(hh)], states[hh]) for hh in heads]
            qs = [_dot(qd_ref[rows, _head_cols(hh)], states[hh]) for hh in heads]
            vn = [u_ref[rows, _head_cols(hh)] - ws[hh] for hh in heads]
            av = [_dot(a_ref[hh, rows, :], vn[hh]) for hh in heads]
            kv = [_dot(kd_ref[rows, _head_cols(hh)], vn[hh], 0, 0) for hh in heads]
            for hh in heads:
                o_ref[rows, _head_cols(hh)] = qs[hh] + av[hh]
            return tuple(states[hh] * gl_ref[hh, ci] + kv[hh] for hh in heads)

        last = lax.fori_loop(0, NCH // SCAN_PARTS, step, tuple(carry_ref[hh] for hh in heads))
        for hh in heads:
            carry_ref[hh] = last[hh]
            ov = o_ref[:, _head_cols(hh)]
            mix_ref[:, _head_cols(hh)] = (ov * _rms_scale(ov) * wn_ref[...] * _silu(z_ref[:, _head_cols(hh)])).astype(BF16)

    hs, a_s, gl_s, st_s, gz_s, mix_s = _scan_specs(SCAN_HEADS, SCAN_PARTS, False)
    return pl.pallas_call(
        body, name="gdn_scan", grid=(NGH // SCAN_HEADS, SCAN_PARTS),
        in_specs=[hs, hs, hs, hs, a_s, gl_s, gz_s, pl.BlockSpec((1, GHD), lambda g, p: (0, 0)),
                  pl.BlockSpec(memory_space=pl.ANY)],
        out_specs=[mix_s, hs, st_s],
        out_shape=[jax.ShapeDtypeStruct((S, D), BF16), jax.ShapeDtypeStruct((S, DGDN), F32),
                   jax.ShapeDtypeStruct((NGH, NCH, GHD, GHD), F32)],
        input_output_aliases={8: 0}, scratch_shapes=[pltpu.VMEM((SCAN_HEADS, GHD, GHD), F32)], compiler_params=_cparams(),
    )(u, w, qd, kd, a, gl, proj, w_norm, mix)


def _gdn_scan_bwd(dmix, o, proj, w_norm, u, w, qd, kd, a, gl, states, dproj, exchange=None):
    ex_in, ex_in_specs, ex_out_specs, ex_out_shape, ex_scratch = _hosted(exchange)
    groups = NGH // SCAN_HEADS_BWD

    def body(*refs):
        dy_ref, o_ref, z_ref, wn_ref, u_ref, w_ref, qd_ref, kd_ref, a_ref, gl_ref, st_ref = refs[:11]
        dz_ref, du_ref, dw_ref, dqd_ref, dkd_ref, da_ref, dgl_ref, dwn_ref = refs[12 + len(ex_in):20 + len(ex_in)]
        do_ref, carry_ref = refs[20 + len(ex_in) + len(ex_out_shape):22 + len(ex_in) + len(ex_out_shape)]
        ex_refs = refs[12:12 + len(ex_in)] + refs[20 + len(ex_in):20 + len(ex_in) + len(ex_out_shape)] + refs[-2:]
        heads = range(SCAN_HEADS_BWD)
        chunks = NCH // SCAN_PARTS_BWD

        if exchange is not None:
            @pl.when((pl.program_id(0) == 0) & (pl.program_id(1) == 0))
            def _():
                exchange.start(*exchange.split(ex_refs))

        @pl.when((pl.program_id(0) == 0) & (pl.program_id(1) == 0))
        def _():
            dwn_ref[...] = jnp.zeros_like(dwn_ref)

        @pl.when(pl.program_id(1) == 0)
        def _():
            carry_ref[...] = jnp.zeros_like(carry_ref)

        wn = wn_ref[...]
        for hh in heads:
            c = _head_cols(hh)
            ov = o_ref[:, c]
            zv = z_ref[:, c]
            g = dy_ref[:, c]
            sig = _sigmoid(zv)
            dz_ref[:, c] = (g * (ov * _rms_scale(ov) * wn) * sig * (1.0 + zv * (1.0 - sig))).astype(BF16)
            do, dwt = _rms_bwd(ov, wn, g * zv * sig)
            do_ref[:, c] = do
            dwn_ref[...] += jnp.sum(dwt, axis=0, keepdims=True)

        def step(t, dstates):
            ci = chunks - 1 - t
            rows = pl.ds(pl.multiple_of(ci * CHUNK, CHUNK), CHUNK)
            cols = [_head_cols(hh) for hh in heads]
            state = [st_ref[hh, ci] for hh in heads]
            dov = [do_ref[rows, cols[hh]] for hh in heads]
            wv = [w_ref[rows, cols[hh]] for hh in heads]
            ws = [_dot(wv[hh], state[hh]) for hh in heads]
            adov = [_dot(a_ref[hh, rows, :], dov[hh], 0, 0) for hh in heads]
            kds = [_dot(kd_ref[rows, cols[hh]], dstates[hh]) for hh in heads]
            dqd = [_dot(dov[hh], state[hh], 1, 1) for hh in heads]
            qdo = [_dot(qd_ref[rows, cols[hh]], dov[hh], 0, 0) for hh in heads]
            vn = [u_ref[rows, cols[hh]] - ws[hh] for hh in heads]
            dvn = [adov[hh] + kds[hh] for hh in heads]
            da = [_dot(dov[hh], vn[hh], 1, 1) for hh in heads]
            dkd = [_dot(vn[hh], dstates[hh], 1, 1) for hh in heads]
            dwv = [_dot(dvn[hh], state[hh], 1, 1) for hh in heads]
            wdv = [_dot(wv[hh], dvn[hh], 0, 0) for hh in heads]
            for hh in heads:
                da_ref[hh, rows, :] = da[hh]
                dqd_ref[rows, cols[hh]] = dqd[hh]
                dkd_ref[rows, cols[hh]] = dkd[hh]
                dgl = jnp.sum(jnp.sum(dstates[hh] * state[hh], axis=1, keepdims=True), axis=0, keepdims=True)
                dgl_ref[hh, ci] = jnp.broadcast_to(dgl, (1, LANES))
                du_ref[rows, cols[hh]] = dvn[hh]
                dw_ref[rows, cols[hh]] = -dwv[hh]
            return tuple(dstates[hh] * gl_ref[hh, ci] + qdo[hh] - wdv[hh] for hh in heads)

        last = lax.fori_loop(0, chunks, step, tuple(carry_ref[hh] for hh in heads))
        for hh in heads:
            carry_ref[hh] = last[hh]

        if exchange is not None:
            @pl.when((pl.program_id(0) == groups - 1) & (pl.program_id(1) == SCAN_PARTS_BWD - 1))
            def _():
                exchange.finish(*exchange.split(ex_refs))

    hs, a_s, gl_s, st_s, gz_s, mix_s = _scan_specs(SCAN_HEADS_BWD, SCAN_PARTS_BWD, True)
    vec = pl.BlockSpec((1, GHD), lambda g, p: (0, 0))
    tok = jax.ShapeDtypeStruct((S, DGDN), F32)
    res = pl.pallas_call(
        body, name="gdn_scan_bwd", grid=(groups, SCAN_PARTS_BWD),
        in_specs=[mix_s, hs, gz_s, vec, hs, hs, hs, hs, a_s, gl_s, st_s, pl.BlockSpec(memory_space=pl.ANY)] + ex_in_specs,
        out_specs=[gz_s, hs, hs, hs, hs, a_s, gl_s, vec] + ex_out_specs,
        out_shape=[jax.ShapeDtypeStruct((S, DPROJ_PAD), BF16), tok, tok, tok, tok,
                   jax.ShapeDtypeStruct((NGH, S, CHUNK), F32), jax.ShapeDtypeStruct((NGH, NCH, 1, LANES), F32),
                   jax.ShapeDtypeStruct((1, GHD), F32)] + ex_out_shape,
        input_output_aliases={11: 0},
        scratch_shapes=[pltpu.VMEM((S // SCAN_PARTS_BWD, SCAN_HEADS_BWD * GHD), F32),
                        pltpu.VMEM((SCAN_HEADS_BWD, GHD, GHD), F32)] + ex_scratch,
        compiler_params=_cparams(),
    )(dmix, o, proj, w_norm, u, w, qd, kd, a, gl, states, dproj, *ex_in)
    return res[:8], res[8:]


def _place():
    return lax.axis_index("x"), lax.axis_index("y"), lax.axis_index("c")


def _other_chips(x, y):
    return [(1 - x, y), (x, 1 - y), (1 - x, 1 - y)]


HBM = pl.BlockSpec(memory_space=pltpu.HBM)
VMEM = pl.BlockSpec(memory_space=pltpu.VMEM)


def _half_rows(ref_or_rows, half):
    rows = ref_or_rows // 2
    return pl.ds(pl.multiple_of(half * rows, rows), rows)


class _Exchange:
    def __init__(self, inputs, out_shape, n_sems, start, finish):
        self.inputs, self.out_shape, self.n_sems, self.start, self.finish = inputs, out_shape, n_sems, start, finish

    def sem_shapes(self):
        return [pltpu.SemaphoreType.DMA((self.n_sems,)), pltpu.SemaphoreType.DMA((self.n_sems,))]

    def split(self, refs):
        n_in, n_out = len(self.inputs), len(self.out_shape)
        return refs[:n_in], refs[n_in:n_in + n_out], refs[n_in + n_out], refs[n_in + n_out + 1]


def _run_exchange(ex, name):
    def body(*refs):
        parts = ex.split(refs)
        ex.start(*parts)
        ex.finish(*parts)

    return pl.pallas_call(
        body, name=name, in_specs=[HBM] * len(ex.inputs), out_specs=[HBM] * len(ex.out_shape), out_shape=ex.out_shape,
        scratch_shapes=ex.sem_shapes(), compiler_params=_cparams(),
    )(*ex.inputs)


def _allgather_exchange(shards, whole=()):
    n, nw = len(shards), len(whole)
    slots = 8

    def plan(src, outs, send_sems, recv_sems):
        x, y, c = _place()
        via_x, via_y, diagonal = _other_chips(x, y)
        id_x, id_y, id_diagonal = [2 * chip[0] + chip[1] for chip in (via_x, via_y, diagonal)]
        me, sibling = (x, y, c), (x, y, 1 - c)

        def rows_of(a, half, quarter):
            total = src[a].shape[0]
            if quarter is None:
                return _half_rows(total, half)
            return pl.ds(pl.multiple_of(half * (total // 2) + quarter * (total // 4), total // 4), total // 4)

        def copy(a, k, chip_index, half, quarter, to, from_src=False):
            rows = rows_of(a, half, quarter)
            dst = outs[a].at[chip_index, rows]
            return pltpu.make_async_remote_copy(
                src_ref=src[a].at[rows] if from_src else dst, dst_ref=dst, send_sem=send_sems.at[slots * a + k],
                recv_sem=recv_sems.at[slots * a + k], device_id=to, device_id_type=MESH)

        def whole_copy(b, k, chip_index, to):
            return pltpu.make_async_remote_copy(
                src_ref=src[n + b], dst_ref=outs[n + b].at[chip_index], send_sem=send_sems.at[slots * n + 3 * b + k],
                recv_sem=recv_sems.at[slots * n + 3 * b + k], device_id=to, device_id_type=MESH)

        first, stages, last = [], [], []
        for a in range(n):
            first += [copy(a, 0, 2 * x + y, c, None, (*via_x, c), True), copy(a, 1, 2 * x + y, c, None, (*via_y, c), True)]
            stages.append([
                (copy(a, 0, id_x, c, None, me),
                 [copy(a, 2, id_x, c, 0, (*via_y, c)), copy(a, 4, id_x, c, None, sibling)]),
                (copy(a, 1, id_y, c, None, me),
                 [copy(a, 3, id_y, c, 1, (*via_x, c)), copy(a, 5, id_y, c, None, sibling)]),
                (copy(a, 2, id_diagonal, c, 0, me), [copy(a, 6, id_diagonal, c, 0, sibling)]),
                (copy(a, 3, id_diagonal, c, 1, me), [copy(a, 7, id_diagonal, c, 1, sibling)]),
            ])
            last += [copy(a, 4, id_x, 1 - c, None, me), copy(a, 5, id_y, 1 - c, None, me),
                     copy(a, 6, id_diagonal, 1 - c, 0, me), copy(a, 7, id_diagonal, 1 - c, 1, me)]
        for b in range(nw):
            for k, (chip, index) in enumerate(((via_x, id_x), (via_y, id_y), (diagonal, id_diagonal))):
                first.append(whole_copy(b, k, 2 * x + y, (*chip, c)))
                last.append(whole_copy(b, k, index, me))
        return first, stages, last

    def start(*refs):
        for cp in plan(*refs)[0]:
            cp.start()

    def finish(*refs):
        first, stages, last = plan(*refs)
        started = list(first)
        for stage in range(4):
            for per_shard in stages:
                lands, onward = per_shard[stage]
                lands.wait_recv()
                for cp in onward:
                    cp.start()
                started += onward
        for cp in last:
            cp.wait_recv()
        for cp in started:
            cp.wait_send()

    out_shape = [jax.ShapeDtypeStruct((NCHIP,) + s.shape, s.dtype) for s in list(shards) + list(whole)]
    return _Exchange(list(shards) + list(whole), out_shape, slots * n + 3 * nw, start, finish)


def _with_own(gathered, own):
    x, y, _ = _place()
    return lax.dynamic_update_index_in_dim(gathered, own, 2 * x + y, axis=0)


def _simple_exchange(inputs, out_shape, copies_of):
    def start(*refs):
        for cp in copies_of(*refs):
            cp.start()

    def finish(*refs):
        for cp in copies_of(*refs):
            cp.wait()

    return _Exchange(list(inputs), out_shape, len(out_shape) * 3, start, finish)


def _pair_exchange(grads):
    def copies_of(src, outs, send_sems, recv_sems):
        x, y, c = _place()
        return [pltpu.make_async_remote_copy(
            src_ref=src[a].at[:, _half_rows(src[a].shape[1], 1 - c)], dst_ref=outs[a], send_sem=send_sems.at[a],
            recv_sem=recv_sems.at[a], device_id=(x, y, 1 - c), device_id_type=MESH) for a in range(len(src))]

    return _simple_exchange(
        grads, [jax.ShapeDtypeStruct((g.shape[0], g.shape[1] // 2, g.shape[2]), g.dtype) for g in grads], copies_of)


def _pair_sum(grads, theirs, name):
    n = len(grads)

    def body(*refs):
        south = lax.axis_index("c") == 0
        for a in range(n):
            g = refs[a][...]
            half = g.shape[0] // 2
            mine = jnp.where(south, g[:half], g[half:])
            refs[2 * n + a][...] = (mine.astype(F32) + refs[n + a][...].astype(F32)).astype(BF16)

    def specs(arrs):
        return [pl.BlockSpec((None,) + g.shape[1:], lambda j: (j, 0, 0)) for g in arrs]

    return pl.pallas_call(
        body, name=name, grid=(NCHIP,), in_specs=specs(grads) + specs(theirs), out_specs=specs(theirs),
        out_shape=[jax.ShapeDtypeStruct(g.shape, BF16) for g in theirs], compiler_params=_cparams(),
    )(*grads, *theirs)


def _chip_exchange(parts):
    n = len(parts)
    slots = 6

    def plan(src, outs, send_sems, recv_sems):
        x, y, c = _place()
        via_x, via_y, diagonal = _other_chips(x, y)
        id_x, id_y, id_diagonal = [2 * chip[0] + chip[1] for chip in (via_x, via_y, diagonal)]
        first, relays, last = [], [], []
        for a in range(n):
            received, passing = outs[a], outs[n + a]
            half = src[a].shape[1] // 2

            def copy(k, source, target, to, a=a):
                return pltpu.make_async_remote_copy(
                    src_ref=source, dst_ref=target, send_sem=send_sems.at[slots * a + k],
                    recv_sem=recv_sems.at[slots * a + k], device_id=(*to, c), device_id_type=MESH)

            first += [copy(0, src[a].at[id_x], received.at[0], via_x), copy(1, src[a].at[id_y], received.at[1], via_y),
                      copy(2, src[a].at[id_diagonal, 0:half], passing.at[0], via_x),
                      copy(3, src[a].at[id_diagonal, half:2 * half], passing.at[1], via_y)]
            relays += [(copy(2, src[a].at[id_diagonal, 0:half], passing.at[0], via_x),
                        copy(4, passing.at[0], received.at[2, 0:half], via_y)),
                       (copy(3, src[a].at[id_diagonal, half:2 * half], passing.at[1], via_y),
                        copy(5, passing.at[1], received.at[2, half:2 * half], via_x))]
            last += [copy(0, src[a].at[id_x], received.at[0], via_x), copy(1, src[a].at[id_y], received.at[1], via_y),
                     copy(4, passing.at[0], received.at[2, 0:half], via_y),
                     copy(5, passing.at[1], received.at[2, half:2 * half], via_x)]
        return first, relays, last

    def start(*refs):
        for cp in plan(*refs)[0]:
            cp.start()

    def finish(*refs):
        first, relays, last = plan(*refs)
        for lands, onward in relays:
            lands.wait_recv()
            onward.start()
        for cp in last:
            cp.wait_recv()
        for cp in first + [onward for _, onward in relays]:
            cp.wait_send()

    out_shape = [jax.ShapeDtypeStruct((NCHIP - 1,) + p.shape[1:], p.dtype) for p in parts]
    out_shape += [jax.ShapeDtypeStruct((2, p.shape[1] // 2, p.shape[2]), p.dtype) for p in parts]
    return _Exchange(list(parts), out_shape, slots * n, start, finish)


def _chip_sum(parts, received, exchange=None):
    n = len(parts)
    steps = 4
    ex_in, ex_in_specs, ex_out_specs, ex_out_shape, ex_scratch = _hosted(exchange)

    def body(*refs):
        ex_refs = refs[2 * n:2 * n + len(ex_in)] + refs[3 * n + len(ex_in):]
        if exchange is not None:
            @pl.when(pl.program_id(0) == 0)
            def _():
                exchange.start(*exchange.split(ex_refs))

        chip = 2 * lax.axis_index("x") + lax.axis_index("y")
        for a in range(n):
            p, r = refs[a], refs[n + a]
            own = jnp.where(chip == 0, p[0], jnp.where(chip == 1, p[1], jnp.where(chip == 2, p[2], p[3])))
            refs[2 * n + len(ex_in) + a][...] = ((own.astype(F32) + r[0].astype(F32)) + r[1].astype(F32)) + r[2].astype(F32)

        if exchange is not None:
            @pl.when(pl.program_id(0) == steps - 1)
            def _():
                exchange.finish(*exchange.split(ex_refs))

    def specs(arrs):
        return [pl.BlockSpec((g.shape[0], g.shape[1] // steps, g.shape[2]), lambda i: (0, i, 0)) for g in arrs]

    out_specs = [pl.BlockSpec((g.shape[1] // steps, g.shape[2]), lambda i: (i, 0)) for g in parts]
    res = pl.pallas_call(
        body, name="grads_chip_sum", grid=(steps,), in_specs=specs(parts) + specs(received) + ex_in_specs,
        out_specs=out_specs + ex_out_specs, out_shape=[jax.ShapeDtypeStruct(g.shape[1:], F32) for g in parts] + ex_out_shape,
        scratch_shapes=ex_scratch, compiler_params=_cparams(),
    )(*parts, *received, *ex_in)
    return res[:n], res[n:]


def _pair_share(halves):
    def copies_of(src, outs, send_sems, recv_sems):
        x, y, c = _place()
        return [pltpu.make_async_remote_copy(
            src_ref=src[a], dst_ref=outs[a], send_sem=send_sems.at[a], recv_sem=recv_sems.at[a],
            device_id=(x, y, 1 - c), device_id_type=MESH) for a in range(len(src))]

    return _simple_exchange(halves, [jax.ShapeDtypeStruct(h.shape, F32) for h in halves], copies_of)


def _adamw_math(w, g, m, v):
    nm = ADAM_B1 * m + (1.0 - ADAM_B1) * g
    nv = ADAM_B2 * v + (1.0 - ADAM_B2) * jnp.square(g)
    m_hat = nm / (1.0 - ADAM_B1 ** ADAM_STEP)
    v_hat = nv / (1.0 - ADAM_B2 ** ADAM_STEP)
    return -ADAM_LR * (m_hat / (jnp.sqrt(v_hat) + ADAM_EPS) + ADAM_WD * w), nm, nv


def _adamw_big(ws, g_mine, g_theirs, ms, vs, exchange=None):
    n = len(ws)
    steps = 8
    ex_in, ex_in_specs, ex_out_specs, ex_out_shape, ex_scratch = _hosted(exchange)

    def body(*refs):
        ex_refs = refs[5 * n:5 * n + len(ex_in)] + refs[9 * n + len(ex_in):]
        outs = refs[5 * n + len(ex_in):9 * n + len(ex_in)]
        if exchange is not None:
            @pl.when(pl.program_id(0) == 0)
            def _():
                exchange.start(*exchange.split(ex_refs))

        own_half = (pl.program_id(0) // (steps // 2)) == lax.axis_index("c")
        for a in range(n):
            g = jnp.where(own_half, refs[n + a][...], refs[2 * n + a][...])
            d, nm, nv = _adamw_math(refs[a][...], g, refs[3 * n + a][...], refs[4 * n + a][...])
            outs[a][...] = g
            outs[n + a][...] = d
            outs[2 * n + a][...] = nm
            outs[3 * n + a][...] = nv

        if exchange is not None:
            @pl.when(pl.program_id(0) == steps - 1)
            def _():
                exchange.finish(*exchange.split(ex_refs))

    specs = [pl.BlockSpec((w.shape[0] // steps, w.shape[1]), lambda i: (i, 0)) for w in ws]
    half_specs = [pl.BlockSpec((g.shape[0] // (steps // 2), g.shape[1]), lambda i: (i % (steps // 2), 0)) for g in g_mine]
    shapes = [jax.ShapeDtypeStruct(w.shape, F32) for w in ws]
    res = pl.pallas_call(
        body, name="adamw_big", grid=(steps,), in_specs=specs + half_specs * 2 + specs * 2 + ex_in_specs,
        out_specs=specs * 4 + ex_out_specs, out_shape=shapes * 4 + ex_out_shape, scratch_shapes=ex_scratch,
        compiler_params=_cparams(),
    )(*ws, *g_mine, *g_theirs, *ms, *vs, *ex_in)
    return res[:n], res[n:2 * n], res[2 * n:3 * n], res[3 * n:4 * n], res[4 * n:]


def _adamw_in(w, g_mine, g_theirs, m, v):
    half = D // 2

    def body(w_ref, gm_ref, gt_ref, m_ref, v_ref, g_out, d_out, nm_out, nv_out, g_ref):
        south = lax.axis_index("c") == 0
        g_ref[0:half, :] = jnp.where(south, gm_ref[...], gt_ref[...])
        g_ref[half:D, :] = jnp.where(south, gt_ref[...], gm_ref[...])
        g = g_ref[0:CW, :]
        d, nm, nv = _adamw_math(w_ref[...], g, m_ref[...], v_ref[...])
        g_out[...] = g
        d_out[...] = d
        nm_out[...] = nm
        nv_out[...] = nv

    spec = pl.BlockSpec((CW, LANES), lambda i: (0, i))
    half_spec = pl.BlockSpec((half, LANES), lambda i: (0, i))
    return pl.pallas_call(
        body, name="adamw_in", grid=(D // LANES,), in_specs=[spec, half_spec, half_spec, spec, spec], out_specs=[spec] * 4,
        out_shape=[jax.ShapeDtypeStruct((CW, D), F32)] * 4, scratch_shapes=[pltpu.VMEM((D, LANES), F32)],
        compiler_params=_cparams(),
    )(w, g_mine, g_theirs, m, v)


NORM_NAMES = ("pre_mix_norm", "post_mix_norm", "pre_mlp_norm", "post_mlp_norm")
SMALL_NAMES = NORM_NAMES + ("gdn_conv_w", "fox_f_bias", "gdn_dt_bias", "gdn_a_log", "fox_out_norm", "gdn_out_norm")
CONV_COLS = 3 * DGDN // NCHIP


def _small_gather(d_norms, d_conv, sums, d_fox_norm, d_gdn_norm, loss_row):
    n_arrays = 6
    n_remote = n_arrays * (NDEV - 1)

    def copies_of(src, outs, send_sems, recv_sems):
        x, y, c = _place()
        me = 4 * x + 2 * y + c

        def from_me(chip_index):
            cols = pl.ds(pl.multiple_of(chip_index * CONV_COLS, LANES), CONV_COLS)
            return [src[0], src[1].at[:, cols], src[2], src[3], src[4], src[5]]

        local = [pltpu.make_async_copy(s, outs[a].at[me], send_sems.at[n_remote + a]) for a, s in enumerate(from_me(2 * x + y))]
        remote = []
        for k in range(1, NDEV):
            px, py, pc = x ^ ((k >> 2) & 1), y ^ ((k >> 1) & 1), c ^ (k & 1)
            remote += [pltpu.make_async_remote_copy(
                src_ref=s, dst_ref=outs[a].at[me], send_sem=send_sems.at[n_arrays * (k - 1) + a],
                recv_sem=recv_sems.at[n_arrays * (k - 1) + a], device_id=(px, py, pc), device_id_type=MESH)
                for a, s in enumerate(from_me(2 * px + py))]
        return local + remote

    def start(*refs):
        for cp in copies_of(*refs):
            cp.start()

    def finish(*refs):
        for cp in copies_of(*refs):
            cp.wait()

    shapes = [(4, D), (CONV_K, CONV_COLS), (8, LANES), (1, LANES), (1, LANES), (1, LANES)]
    return _Exchange([d_norms, d_conv, sums, d_fox_norm, d_gdn_norm, loss_row],
                     [jax.ShapeDtypeStruct((NDEV,) + s, F32) for s in shapes], n_remote + n_arrays, start, finish)


def _small_adamw(gathered, ws, ms, vs):
    n = len(SMALL_NAMES)
    ng = len(gathered)

    def body(*refs):
        def total(buf):
            acc = buf[0]
            for i in range(1, NDEV):
                acc = acc + buf[i]
            return acc

        t_norms, t_conv, t_sums, t_fn, t_gn, t_loss = [total(r) for r in refs[:ng]]
        w_refs, m_refs, v_refs = refs[ng:ng + n], refs[ng + n:ng + 2 * n], refs[ng + 2 * n:ng + 3 * n]
        outs = refs[ng + 3 * n:]
        outs[4 * n][...] = t_loss
        grads = [t_norms[i:i + 1, :] for i in range(4)] + [
            t_conv, t_sums[0:1, 0:NFH], t_sums[1:2, 0:NGH], t_sums[2:3, 0:NGH], t_fn[:, 0:FHD], t_gn]
        for a in range(n):
            d, nm, nv = _adamw_math(w_refs[a][...], grads[a], m_refs[a][...], v_refs[a][...])
            outs[a][...] = grads[a]
            outs[n + a][...] = d
            outs[2 * n + a][...] = nm
            outs[3 * n + a][...] = nv

    def whole(arr):
        return pl.BlockSpec(arr.shape, lambda i: (0,) * arr.ndim)

    res = pl.pallas_call(
        body, name="small_adamw", grid=(1,), in_specs=[whole(t) for t in gathered] + [whole(w) for w in ws] * 3,
        out_specs=[whole(w) for w in ws] * 4 + [pl.BlockSpec((1, LANES), lambda i: (0, 0))],
        out_shape=[jax.ShapeDtypeStruct(w.shape, F32) for w in ws] * 4 + [jax.ShapeDtypeStruct((1, LANES), F32)],
        compiler_params=_cparams(),
    )(*gathered, *ws, *ms, *vs)
    return res[:n], res[n:2 * n], res[2 * n:3 * n], res[3 * n:4 * n], res[4 * n]


CW = DPROJ // NCHIP
PROJ_RUNS = tuple((part * DFOX + hp * LANES, part * DFOX + (hp + 1) * LANES, (3 * hp + part) * LANES)
                  for hp in range(NPAIR) for part in range(3)) + (
    (1536, 1544, BLK_SMALL * LANES), (1544, 3080, BLK_GDN * LANES), (3080, 3088, BLK_SMALL * LANES + 8),
    (3088, 3600, BLK_GZ * LANES))


def _proj_pieces():
    pieces = []
    for lo, hi, at in PROJ_RUNS:
        while lo < hi:
            j = lo // CW
            end = min(hi, (j + 1) * CW)
            pieces.append((j, lo - j * CW, at, end - lo))
            at, lo = at + end - lo, end
    return pieces


RT = 256


def _to_padded_rows(gathered):
    def body(src_ref, out_ref, blocks_ref, rows_ref):
        blocks_ref[...] = src_ref[...].astype(F32)
        rows_ref[...] = jnp.zeros_like(rows_ref)
        for j, start, at, n in _proj_pieces():
            rows_ref[at:at + n, :] = blocks_ref[j, start:start + n, :]
        out_ref[...] = rows_ref[...].astype(out_ref.dtype)

    return pl.pallas_call(
        body, name="proj_rows_in", grid=(D // RT,), in_specs=[pl.BlockSpec((NCHIP, D, RT), lambda i: (0, 0, i))],
        out_specs=pl.BlockSpec((DPROJ_PAD, RT), lambda i: (0, i)), out_shape=jax.ShapeDtypeStruct((DPROJ_PAD, D), gathered.dtype),
        scratch_shapes=[pltpu.VMEM((NCHIP, D, RT), F32), pltpu.VMEM((DPROJ_PAD, RT), F32)], compiler_params=_cparams(),
    )(gathered)


def _from_padded_rows(w):
    def body(src_ref, out_ref, rows_ref, blocks_ref):
        rows_ref[...] = src_ref[...].astype(F32)
        blocks_ref[...] = jnp.zeros_like(blocks_ref)
        for j, start, at, n in _proj_pieces():
            blocks_ref[j, start:start + n, :] = rows_ref[at:at + n, :]
        out_ref[...] = blocks_ref[...].astype(out_ref.dtype)

    return pl.pallas_call(
        body, name="proj_rows_out", grid=(D // RT,), in_specs=[pl.BlockSpec((DPROJ_PAD, RT), lambda i: (0, i))],
        out_specs=pl.BlockSpec((NCHIP, D, RT), lambda i: (0, 0, i)), out_shape=jax.ShapeDtypeStruct((NCHIP, D, D), w.dtype),
        scratch_shapes=[pltpu.VMEM((DPROJ_PAD, RT), F32), pltpu.VMEM((NCHIP, D, RT), F32)], compiler_params=_cparams(),
    )(w)


def _local_step(x, target, first_weights, late_weights, reduce_late, reduce_in, pre_mix_norm, fox_f_bias, fox_out_norm,
                gdn_a_log, gdn_dt_bias, gdn_out_norm, post_mix_norm, pre_mlp_norm, post_mlp_norm):
    bias_vec = jnp.zeros((1, LANES), F32).at[0, 0:NFH].set(fox_f_bias).at[0, LANE_G:LANE_G + NGH].set(gdn_dt_bias)
    alog_vec = jnp.zeros((1, LANES), F32).at[0, LANE_G:LANE_G + NGH].set(gdn_a_log)
    w2 = jnp.concatenate([fox_out_norm, fox_out_norm], axis=1)

    h, first = _pre_norm(x, pre_mix_norm, exchange=first_weights[0])
    win_p, conv_w = first_weights[1](first)
    proj = _matmul(h, win_p, tb=True, tm=2048, tn=768, tk=1024, name="mm_proj", exchange=late_weights[0])
    proj, late_a = proj if late_weights[0] is not None else (proj, [])
    gates = _gates(proj, bias_vec, alog_vec)
    mix, fox_o, lse, late_b = _fox_fwd(proj, gates, w2, exchange=late_weights[1])
    qkv = _gdn_pre(proj, conv_w)
    (u, w, qd, kd, a_intra, gl, t_inv), late_c = _gdn_prep(qkv, gates, exchange=late_weights[2])
    wout, wup3, wdown = late_weights[3](late_a, late_b, late_c)
    mix, gdn_raw, states = _gdn_scan(u, w, qd, kd, a_intra, gl, proj, gdn_out_norm, mix)
    mixed = _matmul(mix, wout, tm=2048, tk=1024, name="mm_out")
    x1, h2 = _post_mix(x, mixed, post_mix_norm, pre_mlp_norm)

    def relu2(acc):
        r = jnp.maximum(acc, 0.0)
        return r, r * r

    up_relu, act = _matmul(h2, wup3, b3=True, tm=1024, tn=1024, tk=1024, out_dtypes=(BF16, BF16), epilogue=relu2,
                           name="mm_up")
    y = _matmul(act, wdown, tm=1024, tk=DFF, name="mm_down")
    dx2, dy, d_post_mlp, loss_row = _loss_head(x1, y, post_mlp_norm, target)

    dwdown = _matmul(act, dy, ta=True, tm=1024, tn=1024, tk=2048, out_dtypes=(BF16,), name="mm_dwdown")

    def relu2_bwd(acc, r):
        return (acc * 2.0 * r.astype(F32),)

    dup = _matmul(dy, wdown, tb=True, tm=1024, tn=1024, tk=1024, out_dtypes=(BF16,), extra=(up_relu,), epilogue=relu2_bwd,
                  name="mm_dact")
    dwup3 = _matmul(h2, dup, ta=True, tm=1024, tn=1024, tk=2048, out_dtypes=(BF16,), o3=True, name="mm_dwup")
    dh2 = _matmul(dup, wup3, tb=True, b3=True, tm=1024, tk=DFF, name="mm_dh2")
    dx1, dmixed, d_pre_mlp, d_post_mix = _mid_bwd(dh2, x1, pre_mlp_norm, dx2, mixed, post_mix_norm)
    dwout = _matmul(mix, dmixed, ta=True, tm=1024, tn=1024, tk=2048, out_dtypes=(BF16,), name="mm_dwout")
    dmix = _matmul(dmixed, wout, tb=True, tm=2048, tk=1024, name="mm_dmix")

    dfox, delta, d_fox_norm, from_sibling = _fox_norm_bwd(fox_o, dmix, w2, exchange=reduce_late[0](dwout, dwup3, dwdown))
    dproj, dcum_fox, reduced_a = _fox_bwd(proj, dfox, gates, lse, delta, exchange=reduce_late[1](from_sibling))
    (dproj, du, dw, dqd, dkd, da, dgl, d_gdn_norm), reduced_b = _gdn_scan_bwd(
        dmix, gdn_raw, proj, gdn_out_norm, u, w, qd, kd, a_intra, gl, states, dproj, exchange=reduce_late[2]())
    dqkv, dgates_gdn, reduced_c = _gdn_prep_bwd(qkv, gates, t_inv, du, dw, dqd, dkd, da, dgl, exchange=reduce_late[3]())
    reduced_late = (reduced_a, reduced_b, reduced_c)
    dproj, d_conv = _gdn_pre_bwd(proj, conv_w, dqkv, dproj)
    dproj, sums = _gates_bwd(proj, bias_vec, alog_vec, dgates_gdn, dcum_fox, dproj)

    dwin_p = _matmul(dproj, h, ta=True, tm=1280, tn=1024, tk=2048, out_dtypes=(BF16,), name="mm_dwin")
    exchange_in = reduce_in(dwin_p)
    dh = _matmul(dproj, win_p, tm=1024, tk=DPROJ_PAD, name="mm_dh", exchange=exchange_in)
    dh, reduced_in = dh if exchange_in is not None else (dh, [])
    grad_x, d_pre_mix = _pre_norm_bwd(dh, x, pre_mix_norm, dx1)

    d_norms = jnp.concatenate([d_pre_mix, d_post_mix, d_pre_mlp, d_post_mlp], axis=0)
    return grad_x, (d_norms, d_conv, sums, d_fox_norm, d_gdn_norm, loss_row), reduced_late, reduced_in


def kernel(x, pre_mix_norm, w_in, fox_f_bias, fox_out_norm, gdn_conv_w, gdn_a_log, gdn_dt_bias, gdn_out_norm, w_out, post_mix_norm, pre_mlp_norm, w_up, w_down, post_mlp_norm, loss_target, m_pre_mix_norm, m_w_in, m_fox_f_bias, m_fox_out_norm, m_gdn_conv_w, m_gdn_a_log, m_gdn_dt_bias, m_gdn_out_norm, m_w_out, m_post_mix_norm, m_pre_mlp_norm, m_w_up, m_w_down, m_post_mlp_norm, v_pre_mix_norm, v_w_in, v_fox_f_bias, v_fox_out_norm, v_gdn_conv_w, v_gdn_a_log, v_gdn_dt_bias, v_gdn_out_norm, v_w_out, v_post_mix_norm, v_pre_mlp_norm, v_w_up, v_w_down, v_post_mlp_norm):
    weights = dict(pre_mix_norm=pre_mix_norm, w_in=w_in, fox_f_bias=fox_f_bias, fox_out_norm=fox_out_norm, gdn_conv_w=gdn_conv_w,
                   gdn_a_log=gdn_a_log, gdn_dt_bias=gdn_dt_bias, gdn_out_norm=gdn_out_norm, w_out=w_out, post_mix_norm=post_mix_norm,
                   pre_mlp_norm=pre_mlp_norm, w_up=w_up, w_down=w_down, post_mlp_norm=post_mlp_norm)
    m_in = dict(pre_mix_norm=m_pre_mix_norm, w_in=m_w_in, fox_f_bias=m_fox_f_bias, fox_out_norm=m_fox_out_norm, gdn_conv_w=m_gdn_conv_w,
                gdn_a_log=m_gdn_a_log, gdn_dt_bias=m_gdn_dt_bias, gdn_out_norm=m_gdn_out_norm, w_out=m_w_out, post_mix_norm=m_post_mix_norm,
                pre_mlp_norm=m_pre_mlp_norm, w_up=m_w_up, w_down=m_w_down, post_mlp_norm=m_post_mlp_norm)
    v_in = dict(pre_mix_norm=v_pre_mix_norm, w_in=v_w_in, fox_f_bias=v_fox_f_bias, fox_out_norm=v_fox_out_norm, gdn_conv_w=v_gdn_conv_w,
                gdn_a_log=v_gdn_a_log, gdn_dt_bias=v_gdn_dt_bias, gdn_out_norm=v_gdn_out_norm, w_out=v_w_out, post_mix_norm=v_post_mix_norm,
                pre_mlp_norm=v_pre_mlp_norm, w_up=v_w_up, w_down=v_w_down, post_mlp_norm=v_post_mlp_norm)
    order_w = ("pre_mix_norm", "w_in", "fox_f_bias", "fox_out_norm", "gdn_conv_w", "gdn_a_log", "gdn_dt_bias", "gdn_out_norm", "w_out",
               "post_mix_norm", "pre_mlp_norm", "w_up", "w_down", "post_mlp_norm")
    big = ("w_in", "w_out", "w_up", "w_down")

    def row(v):
        return v if v.ndim == 2 else v.reshape(1, -1)

    win_shard = jnp.pad(w_in.T.astype(BF16), ((0, D - CW), (0, 0)))

    def resolve_first(gathered):
        win_g, conv_g = gathered
        return (_to_padded_rows(_with_own(win_g, win_shard)),
                _with_own(conv_g, gdn_conv_w).transpose(1, 0, 2).reshape(CONV_K, 3 * DGDN))

    late_shards = [weights[n].astype(BF16) for n in big[1:]]

    def resolve_late(gathered_out, gathered_mlp, _):
        wout_g, wup3, wdown_g = [_with_own(g, own) for g, own in zip(list(gathered_out) + list(gathered_mlp), late_shards)]
        return wout_g.reshape(D, D), wup3, wdown_g.reshape(DFF, D)

    pair_sums, late_blocks = {}, []

    def pair_summed(names, blocks, theirs):
        for n, s in zip(names, _pair_sum(blocks, theirs, "grads_pair_sum_" + names[0])):
            pair_sums[n] = s

    def late_pair_exchange(dwout, dwup3, dwdown):
        late_blocks.extend([dwout.reshape(NCHIP, D // NCHIP, D), dwup3, dwdown.reshape(NCHIP, DFF // NCHIP, D)])
        return _pair_exchange(late_blocks)

    def late_chip_exchange(theirs):
        pair_summed(big[1:], late_blocks, theirs)
        return _chip_exchange([pair_sums["w_up"], pair_sums["w_down"]])

    def reduce_in(dwin_p):
        blocks = [_from_padded_rows(dwin_p)]
        pair_summed(big[:1], blocks, _run_exchange(_pair_exchange(blocks), "grads_pair_exchange_w_in"))
        return _chip_exchange([pair_sums["w_in"]])

    grad_x, small, received_late, received_in = _local_step(
        x[0], loss_target[0], (_allgather_exchange([win_shard], whole=[gdn_conv_w]), resolve_first),
        (_allgather_exchange(late_shards[:1]), _allgather_exchange(late_shards[1:]), None, resolve_late),
        (late_pair_exchange, late_chip_exchange, lambda: None, lambda: _chip_exchange([pair_sums["w_out"]])),
        reduce_in, row(pre_mix_norm), fox_f_bias, row(fox_out_norm), gdn_a_log, gdn_dt_bias,
        row(gdn_out_norm), row(post_mix_norm), row(pre_mlp_norm), row(post_mlp_norm))
    received_mlp, _, received_out = received_late

    g_mine, small_gathered = _chip_sum(
        [pair_sums[n] for n in big], list(received_in[:1]) + list(received_out[:1]) + list(received_mlp[:2]),
        exchange=_small_gather(*small))
    g_theirs = _run_exchange(_pair_share(g_mine), "grads_pair_share")

    g_big, d_big, nm_big, nv_big, _ = _adamw_big(
        [weights[n] for n in big[1:]], g_mine[1:], g_theirs[1:], [m_in[n] for n in big[1:]], [v_in[n] for n in big[1:]])
    in_t = _adamw_in(w_in.T, g_mine[0], g_theirs[0], m_w_in.T, v_w_in.T)
    g_small, d_small, nm_small, nv_small, loss_total = _small_adamw(
        small_gathered, [row(weights[n]) for n in SMALL_NAMES], [row(m_in[n]) for n in SMALL_NAMES],
        [row(v_in[n]) for n in SMALL_NAMES])

    grads, delta, new_m, new_v = {}, {}, {}, {}
    grads["w_in"], delta["w_in"], new_m["w_in"], new_v["w_in"] = [t.T for t in in_t]
    for i, n in enumerate(big[1:]):
        grads[n], delta[n], new_m[n], new_v[n] = g_big[i], d_big[i], nm_big[i], nv_big[i]
    for i, n in enumerate(SMALL_NAMES):
        shape = weights[n].shape
        grads[n], delta[n], new_m[n], new_v[n] = (g_small[i].reshape(shape), d_small[i].reshape(shape),
                                                  nm_small[i].reshape(shape), nv_small[i].reshape(shape))
    return (loss_total[0, 0], grad_x[None], *[grads[n] for n in order_w], *[delta[n] for n in order_w], *[new_m[n] for n in order_w],
            *[new_v[n] for n in order_w])
```

```python
import jax
import jax.numpy as jnp
from jax import lax
from jax.experimental import pallas as pl
from jax.experimental.pallas import tpu as pltpu

F32 = jnp.float32
BF16 = jnp.bfloat16
MESH = pl.DeviceIdType.MESH

S = 2048
D = 1024
NFH, FHD = 8, 64
NPAIR = NFH // 2
NGH, GHD = 4, 128
DFOX = NFH * FHD
DGDN = NGH * GHD
CHUNK = 64
NCH = S // CHUNK
CONV_K = 4
DFF = 4 * D
EPS = 1e-6
DPROJ = 3600
LANES = 128
DPROJ_PAD = 3840
BLK_GDN = 12
BLK_GZ = 24
BLK_SMALL = 28
NCHIP = 4
NDEV = 8
VMEM_LIMIT = 56 * 1024 * 1024

ADAM_LR = 0.001
ADAM_B1 = 0.9
ADAM_B2 = 0.999
ADAM_EPS = 1e-08
ADAM_WD = 0.01
ADAM_STEP = 10


def _cparams(**kw):
    return pltpu.CompilerParams(vmem_limit_bytes=VMEM_LIMIT, **kw)


def _dn(ca, cb):
    return (((ca,), (cb,)), ((), ()))


def _dot(a, b, ca=1, cb=0):
    return lax.dot_general(a.astype(BF16), b.astype(BF16), _dn(ca, cb), preferred_element_type=F32)


def _hdot(a, b, ca=1, cb=0):
    return lax.dot_general(a.astype(F32), b.astype(F32), _dn(ca, cb), precision=lax.Precision.HIGHEST,
                           preferred_element_type=F32)


def _dot3(a, b, ca=1, cb=0):
    a_hi, b_hi = a.astype(BF16), b.astype(BF16)
    a_lo, b_lo = (a - a_hi.astype(F32)).astype(BF16), (b - b_hi.astype(F32)).astype(BF16)
    dn = _dn(ca, cb)
    return (lax.dot_general(a_hi, b_hi, dn, preferred_element_type=F32)
            + (lax.dot_general(a_hi, b_lo, dn, preferred_element_type=F32)
               + lax.dot_general(a_lo, b_hi, dn, preferred_element_type=F32)))


@jax.custom_vjp
def _mm_nn(a, b):
    return _dot(a, b, 1, 0)


def _mm_nn_fwd(a, b):
    return _dot(a, b, 1, 0), (a, b)


def _mm_nn_bwd(res, g):
    a, b = res
    return _dot(g, b, 1, 1), _dot(a, g, 0, 0)


_mm_nn.defvjp(_mm_nn_fwd, _mm_nn_bwd)


@jax.custom_vjp
def _mm_nt(a, b):
    return _dot(a, b, 1, 1)


def _mm_nt_fwd(a, b):
    return _dot(a, b, 1, 1), (a, b)


def _mm_nt_bwd(res, g):
    a, b = res
    return _dot(g, b, 1, 0), _dot(g, a, 0, 0)


_mm_nt.defvjp(_mm_nt_fwd, _mm_nt_bwd)


@jax.custom_vjp
def _saved_inverse(m, t_inv):
    del m
    return t_inv


def _saved_inverse_fwd(m, t_inv):
    del m
    return t_inv, t_inv


def _saved_inverse_bwd(t_inv, g):
    return -_dot3(_dot3(t_inv, g, 0, 0), t_inv, 1, 1), jnp.zeros_like(t_inv)


_saved_inverse.defvjp(_saved_inverse_fwd, _saved_inverse_bwd)


def _sigmoid(z):
    return 1.0 / (1.0 + jnp.exp(-z))


def _softplus(z):
    return jnp.maximum(z, 0.0) + jnp.log(1.0 + jnp.exp(-jnp.abs(z)))


def _silu(z):
    return z * _sigmoid(z)


def _rms_scale(x):
    return lax.rsqrt(jnp.mean(x * x, axis=-1, keepdims=True) + EPS)


def _rms_bwd(x, w, g):
    r = _rms_scale(x)
    gw = g * w
    dx = r * gw - x * (r * r * r) * jnp.mean(gw * x, axis=-1, keepdims=True)
    return dx, g * x * r


def _matmul(a, b, *, name, ta=False, tb=False, tm=512, tn=512, tk=512, out_dtypes=(F32,), b3=False, o3=False,
            extra=(), epilogue=None, exchange=None):
    m, k = (a.shape[1], a.shape[0]) if ta else a.shape
    if b3:
        n = b.shape[1] if tb else b.shape[0] * b.shape[2]
        kb = b.shape[0] * b.shape[2] if tb else b.shape[1]
    else:
        n, kb = (b.shape[0], b.shape[1]) if tb else (b.shape[1], b.shape[0])
    assert kb == k, (name, kb, k)
    tm, tn, tk = min(tm, m), min(tn, n), min(tk, k)
    assert m % tm == 0 and n % tn == 0 and k % tk == 0, (name, m, n, k, tm, tn, tk)
    nk = k // tk
    whole_k_blocks = b3 and tb and not ta and nk == 1 and b.shape[0] > 1
    n_extra = len(extra)
    n_out = len(out_dtypes)
    grid = (m // tm, n // tn, nk)
    ex_in, ex_in_specs, ex_out_specs, ex_out_shape, ex_scratch = _hosted(exchange)

    def body(*refs):
        a_ref, b_ref = refs[0], refs[1]
        extra_refs = refs[2:2 + n_extra]
        first_out = 2 + n_extra + len(ex_in)
        out_refs = refs[first_out:first_out + n_out]
        ex_refs = refs[2 + n_extra:first_out] + refs[first_out + n_out:first_out + n_out + len(ex_out_shape)] + refs[-2:]
        step = [pl.program_id(d) for d in range(3)]

        if exchange is not None:
            @pl.when((step[0] == 0) & (step[1] == 0) & (step[2] == 0))
            def _():
                exchange.start(*exchange.split(ex_refs))

        def finish(acc):
            outs = (acc,) if epilogue is None else epilogue(acc, *[r[...] for r in extra_refs])
            for o_ref, val in zip(out_refs, outs):
                o_ref[...] = val.astype(o_ref.dtype)

        if whole_k_blocks:
            width = b.shape[2]
            part = _dot(a_ref[:, 0:width], b_ref[0], 1, 1)
            for blk in range(1, b.shape[0]):
                part = part + _dot(a_ref[:, blk * width:(blk + 1) * width], b_ref[blk], 1, 1)
        else:
            part = _dot(a_ref[...], b_ref[...], 0 if ta else 1, 1 if tb else 0)
        if nk == 1:
            finish(part)
        else:
            acc_ref = refs[first_out + n_out + len(ex_out_shape)]

            @pl.when(step[2] == 0)
            def _():
                acc_ref[...] = part

            @pl.when(step[2] > 0)
            def _():
                acc_ref[...] += part

            @pl.when(step[2] == nk - 1)
            def _():
                finish(acc_ref[...])

        if exchange is not None:
            flat = (step[0] * grid[1] + step[1]) * nk + step[2]
            total = grid[0] * grid[1] * nk

            @pl.when(flat == total // 2)
            def _():
                exchange.middle(*exchange.split(ex_refs))

            @pl.when(flat == total - 1)
            def _():
                exchange.rest(*exchange.split(ex_refs))

    a_spec = pl.BlockSpec((tk, tm), lambda i, j, kk: (kk, i)) if ta else pl.BlockSpec((tm, tk), lambda i, j, kk: (i, kk))
    if whole_k_blocks:
        b_spec = pl.BlockSpec((b.shape[0], tn, b.shape[2]), lambda i, j, kk: (0, j, 0))
    elif b3 and tb:
        assert b.shape[2] == tk
        b_spec = pl.BlockSpec((None, tn, tk), lambda i, j, kk: (kk, j, 0))
    elif b3:
        assert b.shape[2] == tn
        b_spec = pl.BlockSpec((None, tk, tn), lambda i, j, kk: (j, kk, 0))
    elif tb:
        b_spec = pl.BlockSpec((tn, tk), lambda i, j, kk: (j, kk))
    else:
        b_spec = pl.BlockSpec((tk, tn), lambda i, j, kk: (kk, j))
    tile = pl.BlockSpec((tm, tn), lambda i, j, kk: (i, j))
    out_specs = [tile] * n_out
    out_shape = [jax.ShapeDtypeStruct((m, n), dt) for dt in out_dtypes]
    if o3:
        out_specs[0] = pl.BlockSpec((None, tm, tn), lambda i, j, kk: (j, i, 0))
        out_shape[0] = jax.ShapeDtypeStruct((n // tn, m, tn), out_dtypes[0])
    res = pl.pallas_call(
        body, name=name, grid=grid,
        in_specs=[a_spec, b_spec] + [tile] * n_extra + ex_in_specs, out_specs=out_specs + ex_out_specs,
        out_shape=out_shape + ex_out_shape,
        scratch_shapes=([pltpu.VMEM((tm, tn), F32)] if nk > 1 else []) + ex_scratch,
        compiler_params=_cparams(),
    )(a, b, *extra, *ex_in)
    if exchange is not None:
        return (res[0] if n_out == 1 else res[:n_out]), res[n_out:]
    return res[0] if n_out == 1 else res


TR = 256


def _row_spec(cols):
    return pl.BlockSpec((TR, cols), lambda i: (i, 0))


def _vec_spec(cols):
    return pl.BlockSpec((1, cols), lambda i: (0, 0))


def _pre_norm(x, w, exchange=None):
    ex_in, ex_in_specs, ex_out_specs, ex_out_shape, ex_scratch = _hosted(exchange)

    def body(*refs):
        x_ref, w_ref, h_ref = refs[0], refs[1], refs[2 + len(ex_in)]
        ex_refs = refs[2:2 + len(ex_in)] + refs[3 + len(ex_in):]
        if exchange is not None:
            @pl.when(pl.program_id(0) == 0)
            def _():
                exchange.start(*exchange.split(ex_refs))

        xv = x_ref[...]
        h_ref[...] = (xv * _rms_scale(xv) * w_ref[...]).astype(BF16)

        if exchange is not None:
            @pl.when(pl.program_id(0) == S // TR - 1)
            def _():
                exchange.finish(*exchange.split(ex_refs))

    res = pl.pallas_call(
        body, name="pre_norm", grid=(S // TR,), in_specs=[_row_spec(D), _vec_spec(D)] + ex_in_specs,
        out_specs=[_row_spec(D)] + ex_out_specs, out_shape=[jax.ShapeDtypeStruct((S, D), BF16)] + ex_out_shape,
        scratch_shapes=ex_scratch, compiler_params=_cparams(),
    )(x, w, *ex_in)
    return res[0], res[1:]


def _post_mix(x, mixed, w_post, w_pre_mlp):
    def body(x_ref, m_ref, wp_ref, wm_ref, x1_ref, h2_ref):
        mv = m_ref[...]
        x1 = x_ref[...] + mv * _rms_scale(mv) * wp_ref[...]
        x1_ref[...] = x1
        h2_ref[...] = (x1 * _rms_scale(x1) * wm_ref[...]).astype(BF16)

    return pl.pallas_call(
        body, name="post_mix", grid=(S // TR,),
        in_specs=[_row_spec(D), _row_spec(D), _vec_spec(D), _vec_spec(D)], out_specs=[_row_spec(D), _row_spec(D)],
        out_shape=[jax.ShapeDtypeStruct((S, D), F32), jax.ShapeDtypeStruct((S, D), BF16)], compiler_params=_cparams(),
    )(x, mixed, w_post, w_pre_mlp)


def _loss_head(x1, y, w_post_mlp, target):
    def body(x1_ref, y_ref, w_ref, t_ref, dx2_ref, dy_ref, dw_ref, loss_ref):
        i = pl.program_id(0)
        yv = y_ref[...]
        w = w_ref[...]
        x2 = x1_ref[...] + yv * _rms_scale(yv) * w
        err = x2 - t_ref[...]
        dx2 = err * (1.0 / D)
        dx2_ref[...] = dx2
        dy, dwt = _rms_bwd(yv, w, dx2)
        dy_ref[...] = dy.astype(BF16)

        @pl.when(i == 0)
        def _():
            dw_ref[...] = jnp.zeros_like(dw_ref)
            loss_ref[...] = jnp.zeros_like(loss_ref)

        dw_ref[...] += jnp.sum(dwt, axis=0, keepdims=True)
        part = 0.5 * jnp.sum(jnp.mean(err * err, axis=-1, keepdims=True), axis=0, keepdims=True)
        loss_ref[...] += jnp.broadcast_to(part, loss_ref.shape)

    return pl.pallas_call(
        body, name="loss_head", grid=(S // TR,),
        in_specs=[_row_spec(D), _row_spec(D), _vec_spec(D), _row_spec(D)],
        out_specs=[_row_spec(D), _row_spec(D), _vec_spec(D), _vec_spec(LANES)],
        out_shape=[jax.ShapeDtypeStruct((S, D), F32), jax.ShapeDtypeStruct((S, D), BF16),
                   jax.ShapeDtypeStruct((1, D), F32), jax.ShapeDtypeStruct((1, LANES), F32)],
        compiler_params=_cparams(),
    )(x1, y, w_post_mlp, target)


def _mid_bwd(dh2, x1, w_pre_mlp, dx2, mixed, w_post):
    def body(dh2_ref, x1_ref, wm_ref, dx2_ref, m_ref, wp_ref, dx1_ref, dm_ref, dwm_ref, dwp_ref):
        i = pl.program_id(0)
        dxa, dwm = _rms_bwd(x1_ref[...], wm_ref[...], dh2_ref[...])
        dx1 = dx2_ref[...] + dxa
        dx1_ref[...] = dx1
        dm, dwp = _rms_bwd(m_ref[...], wp_ref[...], dx1)
        dm_ref[...] = dm.astype(BF16)

        @pl.when(i == 0)
        def _():
            dwm_ref[...] = jnp.zeros_like(dwm_ref)
            dwp_ref[...] = jnp.zeros_like(dwp_ref)

        dwm_ref[...] += jnp.sum(dwm, axis=0, keepdims=True)
        dwp_ref[...] += jnp.sum(dwp, axis=0, keepdims=True)

    return pl.pallas_call(
        body, name="mid_bwd", grid=(S // TR,),
        in_specs=[_row_spec(D), _row_spec(D), _vec_spec(D), _row_spec(D), _row_spec(D), _vec_spec(D)],
        out_specs=[_row_spec(D), _row_spec(D), _vec_spec(D), _vec_spec(D)],
        out_shape=[jax.ShapeDtypeStruct((S, D), F32), jax.ShapeDtypeStruct((S, D), BF16),
                   jax.ShapeDtypeStruct((1, D), F32), jax.ShapeDtypeStruct((1, D), F32)],
        compiler_params=_cparams(),
    )(dh2, x1, w_pre_mlp, dx2, mixed, w_post)


def _pre_norm_bwd(dh, x, w, dx1):
    def body(dh_ref, x_ref, w_ref, dx1_ref, dx_ref, dw_ref):
        i = pl.program_id(0)
        dxa, dwt = _rms_bwd(x_ref[...], w_ref[...], dh_ref[...])
        dx_ref[...] = dx1_ref[...] + dxa

        @pl.when(i == 0)
        def _():
            dw_ref[...] = jnp.zeros_like(dw_ref)

        dw_ref[...] += jnp.sum(dwt, axis=0, keepdims=True)

    return pl.pallas_call(
        body, name="pre_norm_bwd", grid=(S // TR,),
        in_specs=[_row_spec(D), _row_spec(D), _vec_spec(D), _row_spec(D)], out_specs=[_row_spec(D), _vec_spec(D)],
        out_shape=[jax.ShapeDtypeStruct((S, D), F32), jax.ShapeDtypeStruct((1, D), F32)], compiler_params=_cparams(),
    )(dh, x, w, dx1)


BQ = 512
NQ = S // BQ
LANE_BETA, LANE_G = 8, 12


def _gate_lanes(shape):
    lane = lax.broadcasted_iota(jnp.int32, shape, 1)
    return lane < LANE_BETA, (lane >= LANE_BETA) & (lane < LANE_G), (lane >= LANE_G) & (lane < LANE_G + NGH)


def _gates(proj, bias_vec, alog_vec):
    def body(s_ref, b_ref, a_ref, o_ref, carry_ref):
        i = pl.program_id(0)

        @pl.when(i == 0)
        def _():
            carry_ref[...] = jnp.zeros_like(carry_ref)

        z = s_ref[...] + b_ref[...]
        tail = jnp.log(1.0 + jnp.exp(-jnp.abs(z)))
        sp = jnp.maximum(z, 0.0) + tail
        lf = jnp.minimum(z, 0.0) - tail
        r = lax.broadcasted_iota(jnp.int32, (BQ, BQ), 0)
        c = lax.broadcasted_iota(jnp.int32, (BQ, BQ), 1)
        tri = (c <= r).astype(F32)
        cum = _hdot(tri, lf) + carry_ref[...]
        carry_ref[...] = cum[BQ - 1:BQ, :]
        is_fox, is_beta, is_g = _gate_lanes(z.shape)
        o_ref[...] = jnp.where(is_fox, cum, jnp.where(is_beta, _sigmoid(z), jnp.where(is_g, -jnp.exp(a_ref[...]) * sp, 0.0)))

    return pl.pallas_call(
        body, name="gates", grid=(NQ,),
        in_specs=[pl.BlockSpec((BQ, LANES), lambda i: (i, BLK_SMALL)), _vec_spec(LANES), _vec_spec(LANES)],
        out_specs=pl.BlockSpec((BQ, LANES), lambda i: (i, 0)), out_shape=jax.ShapeDtypeStruct((S, LANES), F32),
        scratch_shapes=[pltpu.VMEM((1, LANES), F32)], compiler_params=_cparams(),
    )(proj, bias_vec, alog_vec)


def _gates_bwd(proj, bias_vec, alog_vec, dgates_gdn, dcum_fox, dproj):
    def body(s_ref, b_ref, a_ref, dg_ref, dc_ref, dproj_in, dproj_ref, red_ref, carry_ref):
        del dproj_in
        i = pl.program_id(0)

        @pl.when(i == 0)
        def _():
            carry_ref[...] = jnp.zeros_like(carry_ref)
            red_ref[...] = jnp.zeros_like(red_ref)

        z = s_ref[...] + b_ref[...]
        dg = dg_ref[...] + dc_ref[...]
        r = lax.broadcasted_iota(jnp.int32, (BQ, BQ), 0)
        c = lax.broadcasted_iota(jnp.int32, (BQ, BQ), 1)
        upper = (c >= r).astype(F32)
        dlf = _hdot(upper, dg) + carry_ref[...]
        carry_ref[...] = dlf[0:1, :]
        sig = _sigmoid(z)
        g_scale = -jnp.exp(a_ref[...])
        is_fox, is_beta, is_g = _gate_lanes(z.shape)
        ds = jnp.where(is_fox, dlf * (1.0 - sig), jnp.where(is_beta, dg * sig * (1.0 - sig), jnp.where(is_g, dg * g_scale * sig, 0.0)))
        dproj_ref[:, 0:LANES] = ds.astype(BF16)
        dproj_ref[:, LANES:2 * LANES] = jnp.zeros((BQ, LANES), BF16)
        dalog = jnp.where(is_g, dg * g_scale * _softplus(z), 0.0)
        sums = jnp.sum(ds, axis=0, keepdims=True)
        red_ref[0:1, :] += jnp.where(is_fox[0:1], sums, 0.0)
        red_ref[1:2, :] += pltpu.roll(jnp.where(is_g[0:1], sums, 0.0), LANES - LANE_G, 1)
        red_ref[2:3, :] += pltpu.roll(jnp.sum(dalog, axis=0, keepdims=True), LANES - LANE_G, 1)

    blk = pl.BlockSpec((BQ, LANES), lambda i: (NQ - 1 - i, 0))
    return pl.pallas_call(
        body, name="gates_bwd", grid=(NQ,),
        in_specs=[pl.BlockSpec((BQ, LANES), lambda i: (NQ - 1 - i, BLK_SMALL)), _vec_spec(LANES), _vec_spec(LANES), blk, blk,
                  pl.BlockSpec(memory_space=pl.ANY)],
        out_specs=[pl.BlockSpec((BQ, 2 * LANES), lambda i: (NQ - 1 - i, BLK_SMALL // 2)), pl.BlockSpec((8, LANES), lambda i: (0, 0))],
        out_shape=[jax.ShapeDtypeStruct((S, DPROJ_PAD), BF16), jax.ShapeDtypeStruct((8, LANES), F32)],
        input_output_aliases={5: 0},
        scratch_shapes=[pltpu.VMEM((1, LANES), F32)], compiler_params=_cparams(),
    )(proj, bias_vec, alog_vec, dgates_gdn, dcum_fox, dproj)


FOX_SCALE = FHD ** -0.5
FOX_PAIRS = 2
FOX_PAIRS_BWD = 2


def _head_mask(e):
    lane = lax.broadcasted_iota(jnp.int32, (1, LANES), 1)
    return (lane >= e * FHD) & (lane < (e + 1) * FHD)


def _lane_col(vals, index):
    lane = lax.broadcasted_iota(jnp.int32, vals.shape, 1)
    return jnp.sum(jnp.where(lane == index, vals, 0.0), axis=1, keepdims=True)


def _sublane_row(vals, index):
    row = lax.broadcasted_iota(jnp.int32, vals.shape, 0)
    return jnp.sum(jnp.where(row == index, vals, 0.0), axis=0, keepdims=True)


def _pair_cols(c0, c1):
    lane = lax.broadcasted_iota(jnp.int32, (c0.shape[0], 2), 1)
    return jnp.where(lane == 0, c0, c1)


def _split3(x):
    hi = x.astype(BF16).astype(F32)
    rest = x - hi
    mid = rest.astype(BF16).astype(F32)
    return hi, mid, (rest - mid).astype(BF16).astype(F32)


def _fox_operand(vals, e, cum, is_query):
    lane = lax.broadcasted_iota(jnp.int32, (1, LANES), 1)
    base = (1 - e) * FHD
    parts = _split3(cum)
    own = jnp.where(_head_mask(e), vals * FOX_SCALE if is_query else vals, 0.0)
    cum_at, ones_at = (base, base + 3) if is_query else (base + 3, base)
    sign = 1.0 if is_query else -1.0
    out = own + jnp.where((lane >= ones_at) & (lane < ones_at + 3), 1.0, 0.0)
    for i, part in enumerate(parts):
        out = out + jnp.where(lane == cum_at + i, sign * part, 0.0)
    return out.astype(BF16)


def _causal_block():
    return lax.broadcasted_iota(jnp.int32, (BQ, BQ), 1) <= lax.broadcasted_iota(jnp.int32, (BQ, BQ), 0)


def _head_rms(o, masks):
    o2 = o * o
    r = [lax.rsqrt(jnp.sum(jnp.where(mk, o2, 0.0), axis=1, keepdims=True) * (1.0 / FHD) + EPS) for mk in masks]
    return jnp.where(masks[0], r[0], r[1])


def _hosted(exchange):
    if exchange is None:
        return [], [], [], [], []
    return (exchange.inputs, [HBM] * len(exchange.inputs), [HBM] * len(exchange.out_shape), exchange.out_shape,
            exchange.sem_shapes())


def _fox_fwd(proj, gates, w2, exchange=None):
    ex_in, ex_in_specs, ex_out_specs, ex_out_shape, ex_scratch = _hosted(exchange)

    n_in = 3 * FOX_PAIRS + 2
    heads = [(pp, e) for pp in range(FOX_PAIRS) for e in range(2)]

    def body(*refs):
        qkv_refs, g_ref, w_ref = refs[:3 * FOX_PAIRS], refs[3 * FOX_PAIRS], refs[3 * FOX_PAIRS + 1]
        mix_ref, o_ref, lse_ref = refs[n_in + len(ex_in):n_in + 3 + len(ex_in)]
        ka_ref, vb_ref = refs[n_in + 3 + len(ex_in) + len(ex_out_shape):n_in + 5 + len(ex_in) + len(ex_out_shape)]
        ex_refs = refs[n_in:n_in + len(ex_in)] + refs[n_in + 3 + len(ex_in):n_in + 3 + len(ex_in) + len(ex_out_shape)] + refs[-2:]
        grp, qi = pl.program_id(0), pl.program_id(1)

        def head_index(pp, e):
            return 2 * (FOX_PAIRS * grp + pp) + e

        if exchange is not None:
            @pl.when((grp == 0) & (qi == 0))
            def _():
                exchange.start(*exchange.split(ex_refs))

        @pl.when(qi == 0)
        def _():
            gt = g_ref[...]
            for pp in range(FOX_PAIRS):
                kv = qkv_refs[3 * pp + 1][...]
                for e in range(2):
                    ka_ref[2 * pp + e] = _fox_operand(kv, e, _lane_col(gt, head_index(pp, e)), False)
                vb_ref[pp] = qkv_refs[3 * pp + 2][...].astype(BF16)

        masks = [_head_mask(0), _head_mask(1)]
        gt = g_ref[pl.ds(pl.multiple_of(qi * BQ, BQ), BQ), :]
        qs = [_fox_operand(qkv_refs[3 * pp][...], e, _lane_col(gt, head_index(pp, e)), True) for pp, e in heads]
        n = range(len(heads))

        def block(kj, carry, diagonal):
            rows = pl.ds(pl.multiple_of(kj * BQ, BQ), BQ)
            s = [_dot(qs[i], ka_ref[i, rows, :], 1, 1) for i in n]
            if diagonal:
                s = [jnp.where(_causal_block(), s[i], -jnp.inf) for i in n]
            m_new = [jnp.maximum(carry[i][0], jnp.max(s[i], axis=-1, keepdims=True)) for i in n]
            p = [jnp.exp(s[i] - m_new[i]) for i in n]
            alpha = [jnp.exp(carry[i][0] - m_new[i]) for i in n]
            l_new = [alpha[i] * carry[i][1] + jnp.sum(p[i], axis=-1, keepdims=True) for i in n]
            pv = [_dot(p[i], vb_ref[heads[i][0], rows, :]) for i in n]
            return tuple((m_new[i], l_new[i], alpha[i] * carry[i][2] + pv[i]) for i in n)

        one = (jnp.full((BQ, 1), -jnp.inf, F32), jnp.zeros((BQ, 1), F32), jnp.zeros((BQ, LANES), F32))
        below = lax.fori_loop(0, qi, lambda kj, carry: block(kj, carry, False), (one,) * len(heads))
        done = block(qi, below, True)
        for pp in range(FOX_PAIRS):
            (m0, l0, a0), (m1, l1, a1) = done[2 * pp], done[2 * pp + 1]
            o = jnp.where(masks[0], a0 / l0, a1 / l1)
            cols = slice(pp * LANES, (pp + 1) * LANES)
            o_ref[:, cols] = o
            mix_ref[:, cols] = (o * _head_rms(o, masks) * w_ref[...]).astype(BF16)
            lse_ref[pp] = _pair_cols(m0 + jnp.log(l0), m1 + jnp.log(l1))

        if exchange is not None:
            @pl.when((grp == NPAIR // FOX_PAIRS // 2) & (qi == 0))
            def _():
                exchange.middle(*exchange.split(ex_refs))

            @pl.when((grp == NPAIR // FOX_PAIRS - 1) & (qi == NQ - 1))
            def _():
                exchange.rest(*exchange.split(ex_refs))

    qkv_specs = []
    for pp in range(FOX_PAIRS):
        qkv_specs.append(pl.BlockSpec((BQ, LANES), lambda g, i, pp=pp: (i, 3 * (FOX_PAIRS * g + pp))))
        qkv_specs.append(pl.BlockSpec((S, LANES), lambda g, i, pp=pp: (0, 3 * (FOX_PAIRS * g + pp) + 1)))
        qkv_specs.append(pl.BlockSpec((S, LANES), lambda g, i, pp=pp: (0, 3 * (FOX_PAIRS * g + pp) + 2)))
    blk = pl.BlockSpec((BQ, FOX_PAIRS * LANES), lambda g, i: (i, g))
    res = pl.pallas_call(
        body, name="fox_fwd", grid=(NPAIR // FOX_PAIRS, NQ),
        in_specs=qkv_specs + [pl.BlockSpec((S, LANES), lambda g, i: (0, 0)), pl.BlockSpec((1, LANES), lambda g, i: (0, 0))]
        + ex_in_specs,
        out_specs=[blk, blk, pl.BlockSpec((FOX_PAIRS, BQ, 2), lambda g, i: (g, i, 0))] + ex_out_specs,
        out_shape=[jax.ShapeDtypeStruct((S, D), BF16), jax.ShapeDtypeStruct((S, DFOX), F32),
                   jax.ShapeDtypeStruct((NPAIR, S, 2), F32)] + ex_out_shape,
        scratch_shapes=[pltpu.VMEM((2 * FOX_PAIRS, S, LANES), BF16), pltpu.VMEM((FOX_PAIRS, S, LANES), BF16)] + ex_scratch,
        compiler_params=_cparams(),
    )(*([proj] * (3 * FOX_PAIRS)), gates, w2, *ex_in)
    return res[0], res[1], res[2], res[3:]


def _fox_norm_bwd(o, dmix, w2, exchange=None):
    ex_in, ex_in_specs, ex_out_specs, ex_out_shape, ex_scratch = _hosted(exchange)

    def body(*refs):
        o_ref, g_ref, w_ref = refs[:3]
        do_ref, dl_ref, dw_ref = refs[3 + len(ex_in):6 + len(ex_in)]
        ex_refs = refs[3:3 + len(ex_in)] + refs[6 + len(ex_in):]
        hp, qi = pl.program_id(0), pl.program_id(1)

        if exchange is not None:
            @pl.when((hp == 0) & (qi == 0))
            def _():
                exchange.start(*exchange.split(ex_refs))

        masks = [_head_mask(0), _head_mask(1)]
        ov = o_ref[...]
        g = g_ref[...]
        r = _head_rms(ov, masks)
        gw = g * w_ref[...]
        gwo = gw * ov
        mean = [jnp.sum(jnp.where(mk, gwo, 0.0), axis=1, keepdims=True) * (1.0 / FHD) for mk in masks]
        do = r * gw - ov * (r * r * r) * jnp.where(masks[0], mean[0], mean[1])
        do_ref[...] = do.astype(BF16)
        doo = do * ov
        dl_ref[...] = _pair_cols(*[jnp.sum(jnp.where(mk, doo, 0.0), axis=1, keepdims=True) for mk in masks])

        @pl.when((hp == 0) & (qi == 0))
        def _():
            dw_ref[...] = jnp.zeros_like(dw_ref)

        dw_ref[...] += jnp.sum(g * ov * r, axis=0, keepdims=True)

        @pl.when((hp == NPAIR - 1) & (qi == NQ - 1))
        def _():
            dw = dw_ref[...]
            dw_ref[...] = dw + pltpu.roll(dw, FHD, 1)
            if exchange is not None:
                exchange.finish(*exchange.split(ex_refs))

    blk = pl.BlockSpec((BQ, LANES), lambda hp, i: (i, hp))
    vec = pl.BlockSpec((1, LANES), lambda hp, i: (0, 0))
    res = pl.pallas_call(
        body, name="fox_norm_bwd", grid=(NPAIR, NQ), in_specs=[blk, blk, vec] + ex_in_specs,
        out_specs=[blk, pl.BlockSpec((None, BQ, 2), lambda hp, i: (hp, i, 0)), vec] + ex_out_specs,
        out_shape=[jax.ShapeDtypeStruct((S, DFOX), BF16), jax.ShapeDtypeStruct((NPAIR, S, 2), F32),
                   jax.ShapeDtypeStruct((1, LANES), F32)] + ex_out_shape,
        scratch_shapes=ex_scratch, compiler_params=_cparams(),
    )(o, dmix, w2, *ex_in)
    return res[0], res[1], res[2], res[3:]


def _fox_bwd(proj, do, gates, lse, delta, exchange=None):
    ex_in, ex_in_specs, ex_out_specs, ex_out_shape, ex_scratch = _hosted(exchange)

    pg = FOX_PAIRS_BWD
    n_in = 3 * pg + 4
    heads = [(pp, e) for pp in range(pg) for e in range(2)]

    def body(*refs):
        qkv_refs = refs[:3 * pg]
        do_ref, g_ref, lse_ref, dl_ref = refs[3 * pg:n_in]
        dproj_ref, dc_ref = refs[n_in + len(ex_in):n_in + 2 + len(ex_in)]
        qa_ref, dq_ref = refs[n_in + 2 + len(ex_in) + len(ex_out_shape):n_in + 4 + len(ex_in) + len(ex_out_shape)]
        ex_refs = refs[n_in:n_in + len(ex_in)] + refs[n_in + 2 + len(ex_in):n_in + 2 + len(ex_in) + len(ex_out_shape)] + refs[-2:]
        grp, kj = pl.program_id(0), pl.program_id(1)

        def head_index(pp, e):
            return 2 * (pg * grp + pp) + e

        if exchange is not None:
            @pl.when((grp == 0) & (kj == 0))
            def _():
                exchange.start(*exchange.split(ex_refs))

        @pl.when(kj == 0)
        def _():
            gt = g_ref[...]
            for pp in range(pg):
                qv = qkv_refs[3 * pp][...]
                for e in range(2):
                    qa_ref[2 * pp + e] = _fox_operand(qv, e, _lane_col(gt, head_index(pp, e)), True)
            dq_ref[...] = jnp.zeros_like(dq_ref)

        @pl.when((grp == 0) & (kj == 0))
        def _():
            dc_ref[...] = jnp.zeros_like(dc_ref)

        masks = [_head_mask(0), _head_mask(1)]
        krows = pl.ds(pl.multiple_of(kj * BQ, BQ), BQ)
        gk = g_ref[krows, :]
        kas = [_fox_operand(qkv_refs[3 * pp + 1][...], e, _lane_col(gk, head_index(pp, e)), False) for pp, e in heads]
        vbs = [qkv_refs[3 * pp + 2][...].astype(BF16) for pp in range(pg)]
        lane = lax.broadcasted_iota(jnp.int32, (BQ, LANES), 1)
        n = range(len(heads))

        def block(qi, carry, diagonal):
            dks, dvs, css = carry
            rows = pl.ds(pl.multiple_of(qi * BQ, BQ), BQ)
            qa = [qa_ref[i, rows, :] for i in n]
            s = [_dot(qa[i], kas[i], 1, 1) for i in n]
            if diagonal:
                s = [jnp.where(_causal_block(), s[i], -jnp.inf) for i in n]
            dov = [do_ref[rows, pp * LANES:(pp + 1) * LANES] for pp in range(pg)]
            doe = [jnp.where(masks[e], dov[pp], jnp.zeros_like(dov[pp])) for pp, e in heads]
            lse2 = [lse_ref[pp, rows, :] for pp in range(pg)]
            dl2 = [dl_ref[pp, rows, :] for pp in range(pg)]
            p = [jnp.exp(s[i] - _lane_col(lse2[heads[i][0]], heads[i][1])) for i in n]
            dp = [_dot(doe[i], vbs[heads[i][0]], 1, 1) for i in n]
            ds = [p[i] * (dp[i] - _lane_col(dl2[heads[i][0]], heads[i][1])) for i in n]
            dv_part = [_dot(p[i], doe[i], 0, 0) for i in n]
            dk_part = [_dot(ds[i], jnp.where(masks[heads[i][1]], qa[i], jnp.zeros_like(qa[i])), 0, 0) for i in n]
            dq_part = [jnp.where(masks[heads[i][1]], _dot(ds[i], kas[i]), 0.0) for i in n]
            css = tuple(css[i] + jnp.sum(ds[i], axis=0, keepdims=True) for i in n)
            dc = jnp.zeros((BQ, LANES), F32)
            for i in n:
                dc = dc + jnp.where(lane == head_index(*heads[i]), jnp.sum(ds[i], axis=1, keepdims=True), 0.0)
            for pp in range(pg):
                dq_ref[pp, rows, :] += (dq_part[2 * pp] + dq_part[2 * pp + 1]) * FOX_SCALE
            dc_ref[rows, :] += dc
            dks = tuple(dks[pp] + dk_part[2 * pp] + dk_part[2 * pp + 1] for pp in range(pg))
            dvs = tuple(dvs[pp] + dv_part[2 * pp] + dv_part[2 * pp + 1] for pp in range(pg))
            return dks, dvs, css

        zero = jnp.zeros((BQ, LANES), F32)
        first = block(kj, ((zero,) * pg, (zero,) * pg, (jnp.zeros((1, BQ), F32),) * len(heads)), True)
        dks, dvs, css = lax.fori_loop(kj + 1, NQ, lambda qi, carry: block(qi, carry, False), first)
        r = lax.broadcasted_iota(jnp.int32, (BQ, BQ), 0)
        c = lax.broadcasted_iota(jnp.int32, (BQ, BQ), 1)
        dcol = jnp.zeros((BQ, LANES), F32)
        for i in n:
            col = jnp.sum(jnp.where(r == c, css[i], 0.0), axis=1, keepdims=True)
            dcol = dcol + jnp.where(lane == head_index(*heads[i]), col, 0.0)
        dc_ref[krows, :] -= dcol
        for pp in range(pg):
            base = 3 * pp * LANES
            dproj_ref[krows, base + LANES:base + 2 * LANES] = dks[pp].astype(BF16)
            dproj_ref[krows, base + 2 * LANES:base + 3 * LANES] = dvs[pp].astype(BF16)

        @pl.when(kj == NQ - 1)
        def _():
            for pp in range(pg):
                dproj_ref[:, 3 * pp * LANES:(3 * pp + 1) * LANES] = dq_ref[pp].astype(BF16)

        if exchange is not None:
            @pl.when((grp == NPAIR // pg // 2) & (kj == 0))
            def _():
                exchange.middle(*exchange.split(ex_refs))

            @pl.when((grp == NPAIR // pg - 1) & (kj == NQ - 1))
            def _():
                exchange.rest(*exchange.split(ex_refs))

    qkv_specs = []
    for pp in range(pg):
        qkv_specs.append(pl.BlockSpec((S, LANES), lambda g, j, pp=pp: (0, 3 * (pg * g + pp))))
        qkv_specs.append(pl.BlockSpec((BQ, LANES), lambda g, j, pp=pp: (j, 3 * (pg * g + pp) + 1)))
        qkv_specs.append(pl.BlockSpec((BQ, LANES), lambda g, j, pp=pp: (j, 3 * (pg * g + pp) + 2)))
    pair = pl.BlockSpec((pg, S, 2), lambda g, j: (g, 0, 0))
    res = pl.pallas_call(
        body, name="fox_bwd", grid=(NPAIR // pg, NQ),
        in_specs=qkv_specs + [pl.BlockSpec((S, pg * LANES), lambda g, j: (0, g)), pl.BlockSpec((S, LANES), lambda g, j: (0, 0)),
                              pair, pair] + ex_in_specs,
        out_specs=[pl.BlockSpec((S, 3 * pg * LANES), lambda g, j: (0, g)), pl.BlockSpec((S, LANES), lambda g, j: (0, 0))]
        + ex_out_specs,
        out_shape=[jax.ShapeDtypeStruct((S, DPROJ_PAD), BF16), jax.ShapeDtypeStruct((S, LANES), F32)] + ex_out_shape,
        scratch_shapes=[pltpu.VMEM((2 * pg, S, LANES), BF16), pltpu.VMEM((pg, S, LANES), F32)] + ex_scratch,
        compiler_params=_cparams(),
    )(*([proj] * (3 * pg)), do, gates, lse, delta, *ex_in)
    return res[0], res[1], res[2:]


NQKV = 3 * NGH
GDN_QSCALE = GHD ** -0.5


def _shift_down(x, s):
    if s == 0:
        return x
    row = lax.broadcasted_iota(jnp.int32, x.shape, 0)
    return jnp.where(row >= s, pltpu.roll(x, s, 0), 0.0)


def _shift_up(x, s):
    if s == 0:
        return x
    n = x.shape[0]
    row = lax.broadcasted_iota(jnp.int32, x.shape, 0)
    return jnp.where(row < n - s, pltpu.roll(x, n - s, 0), 0.0)


def _conv_pre(xv, wv):
    pre = xv * wv[CONV_K - 1:CONV_K, :]
    for j in range(CONV_K - 1):
        pre = pre + _shift_down(xv, CONV_K - 1 - j) * wv[j:j + 1, :]
    return pre


def _l2_factors(b):
    return b < 2 * NGH, jnp.where(b < NGH, GDN_QSCALE, 1.0)


def _gdn_pre(proj, conv_w):
    def body(x_ref, w_ref, o_ref):
        b = pl.program_id(0)
        c = _silu(_conv_pre(x_ref[...], w_ref[...]))
        normed, scale = _l2_factors(b)
        rs = lax.rsqrt(jnp.sum(c * c, axis=-1, keepdims=True) + EPS)
        o_ref[...] = c * jnp.where(normed, rs, 1.0) * scale

    return pl.pallas_call(
        body, name="gdn_pre", grid=(NQKV,),
        in_specs=[pl.BlockSpec((S, GHD), lambda b: (0, BLK_GDN + b)), pl.BlockSpec((CONV_K, GHD), lambda b: (0, b))],
        out_specs=pl.BlockSpec((S, GHD), lambda b: (0, b)),
        out_shape=jax.ShapeDtypeStruct((S, NQKV * GHD), F32), compiler_params=_cparams(),
    )(proj, conv_w)


def _gdn_pre_bwd(proj, conv_w, dqkv, dproj):
    def body(x_ref, w_ref, dy_ref, dproj_in, dx_ref, dw_ref):
        del dproj_in
        b = pl.program_id(0)
        xv = x_ref[...]
        wv = w_ref[...]
        pre = _conv_pre(xv, wv)
        sig = _sigmoid(pre)
        c = pre * sig
        normed, scale = _l2_factors(b)
        g = dy_ref[...] * scale
        rs = lax.rsqrt(jnp.sum(c * c, axis=-1, keepdims=True) + EPS)
        dc_n = rs * g - c * (rs * rs * rs) * jnp.sum(g * c, axis=-1, keepdims=True)
        dc = jnp.where(normed, dc_n, g)
        dpre = dc * sig * (1.0 + pre * (1.0 - sig))
        dx = dpre * wv[CONV_K - 1:CONV_K, :]
        for j in range(CONV_K - 1):
            dx = dx + _shift_up(dpre, CONV_K - 1 - j) * wv[j:j + 1, :]
        dx_ref[...] = dx.astype(BF16)
        for j in range(CONV_K):
            dw_ref[j:j + 1, :] = jnp.sum(dpre * _shift_down(xv, CONV_K - 1 - j), axis=0, keepdims=True)

    return pl.pallas_call(
        body, name="gdn_pre_bwd", grid=(NQKV,),
        in_specs=[pl.BlockSpec((S, GHD), lambda b: (0, BLK_GDN + b)), pl.BlockSpec((CONV_K, GHD), lambda b: (0, b)),
                  pl.BlockSpec((None, S, GHD), lambda b: (b // NGH, 0, b % NGH)), pl.BlockSpec(memory_space=pl.ANY)],
        out_specs=[pl.BlockSpec((S, GHD), lambda b: (0, BLK_GDN + b)), pl.BlockSpec((CONV_K, GHD), lambda b: (0, b))],
        out_shape=[jax.ShapeDtypeStruct((S, DPROJ_PAD), BF16), jax.ShapeDtypeStruct((CONV_K, NQKV * GHD), F32)],
        input_output_aliases={3: 0}, compiler_params=_cparams(),
    )(proj, conv_w, dqkv, dproj)


CB = 16
NCB = NCH // CB


def _chunk_prep(qs, ks, vs, gcols, bcols, t_saved=None):
    n = range(len(qs))
    r = lax.broadcasted_iota(jnp.int32, (CHUNK, CHUNK), 0)
    c = lax.broadcasted_iota(jnp.int32, (CHUNK, CHUNK), 1)
    incl = c <= r
    eye = (r == c).astype(F32)
    grow = [jnp.sum(gcols[i] * eye, axis=0, keepdims=True) for i in n]
    gc_col = [jnp.sum(jnp.where(incl, grow[i], 0.0), axis=1, keepdims=True) for i in n]
    gc_row = [jnp.sum(jnp.where(r <= c, gcols[i], 0.0), axis=0, keepdims=True) for i in n]
    decay = [jnp.exp(jnp.where(incl, gc_col[i] - gc_row[i], -jnp.inf)) for i in n]
    kb = [ks[i] * bcols[i] for i in n]
    vb = [vs[i] * bcols[i] for i in n]
    kk = [_mm_nt(kb[i], ks[i]) for i in n]
    m = [jnp.where(c < r, kk[i] * decay[i], 0.0) for i in n]
    if t_saved is None:
        t_inv = [eye - m[i] for i in n]
        p = [_dot3(m[i], m[i]) for i in n]
        for step in range(5):
            t_inv = [t_inv[i] + _dot3(t_inv[i], p[i]) for i in n]
            if step < 4:
                p = [_dot3(p[i], p[i]) for i in n]
    else:
        t_inv = [_saved_inverse(m[i], t_saved[i]) for i in n]
    egc = [jnp.exp(gc_col[i]) for i in n]
    u = [_mm_nn(t_inv[i], vb[i]) for i in n]
    w = [_mm_nn(t_inv[i], kb[i] * egc[i]) for i in n]
    qk = [_mm_nt(qs[i], ks[i]) for i in n]
    gc_last = [gc_col[i][CHUNK - 1:CHUNK, :] for i in n]
    return [(u[i], w[i], qk[i] * decay[i], qs[i] * egc[i], ks[i] * jnp.exp(gc_last[i] - gc_col[i]), jnp.exp(gc_last[i]),
             t_inv[i]) for i in n]


def _prep_specs():
    rows = CB * CHUNK
    qs = pl.BlockSpec((rows, GHD), lambda i, h: (i, h))
    ks = pl.BlockSpec((rows, GHD), lambda i, h: (i, NGH + h))
    vs = pl.BlockSpec((rows, GHD), lambda i, h: (i, 2 * NGH + h))
    gs = pl.BlockSpec((rows, LANES), lambda i, h: (i, 0))
    a_s = pl.BlockSpec((None, rows, CHUNK), lambda i, h: (h, i, 0))
    gl_s = pl.BlockSpec((None, CB, 1, LANES), lambda i, h: (h, i, 0, 0))
    return qs, ks, vs, gs, a_s, gl_s


def _gdn_prep(qkv, gates, exchange=None):
    ex_in, ex_in_specs, ex_out_specs, ex_out_shape, ex_scratch = _hosted(exchange)

    def body(*refs):
        q_ref, k_ref, v_ref, g_ref = refs[:4]
        u_ref, w_ref, qd_ref, kd_ref, a_ref, gl_ref, t_ref = refs[4 + len(ex_in):11 + len(ex_in)]
        ex_refs = refs[4:4 + len(ex_in)] + refs[11 + len(ex_in):]
        h = pl.program_id(1)

        if exchange is not None:
            @pl.when((pl.program_id(0) == 0) & (h == 0))
            def _():
                exchange.start(*exchange.split(ex_refs))

        chunks = [pl.ds(cidx * CHUNK, CHUNK) for cidx in range(CB)]
        gts = [g_ref[rows, :] for rows in chunks]
        outs = _chunk_prep([q_ref[rows, :] for rows in chunks], [k_ref[rows, :] for rows in chunks],
                           [v_ref[rows, :] for rows in chunks], [_lane_col(gt, LANE_G + h) for gt in gts],
                           [_lane_col(gt, LANE_BETA + h) for gt in gts])
        for cidx, rows in enumerate(chunks):
            u, w, a, qd, kd, gl, t_inv = outs[cidx]
            u_ref[rows, :] = u
            w_ref[rows, :] = w
            qd_ref[rows, :] = qd
            kd_ref[rows, :] = kd
            a_ref[rows, :] = a
            t_ref[rows, :] = t_inv
            gl_ref[cidx] = jnp.broadcast_to(gl, (1, LANES))

        if exchange is not None:
            @pl.when((pl.program_id(0) == NCB - 1) & (h == NGH - 1))
            def _():
                exchange.finish(*exchange.split(ex_refs))

    qs, ks, vs, gs, a_s, gl_s = _prep_specs()
    tok = jax.ShapeDtypeStruct((S, DGDN), F32)
    sq = jax.ShapeDtypeStruct((NGH, S, CHUNK), F32)
    res = pl.pallas_call(
        body, name="gdn_prep", grid=(NCB, NGH), in_specs=[qs, ks, vs, gs] + ex_in_specs,
        out_specs=[qs, qs, qs, qs, a_s, gl_s, a_s] + ex_out_specs,
        out_shape=[tok, tok, tok, tok, sq, jax.ShapeDtypeStruct((NGH, NCH, 1, LANES), F32), sq] + ex_out_shape,
        scratch_shapes=ex_scratch, compiler_params=_cparams(),
    )(qkv, qkv, qkv, gates, *ex_in)
    return res[:7], res[7:]


def _gdn_prep_bwd(qkv, gates, t_inv, du, dw, dqd, dkd, da, dgl, exchange=None):
    ex_in, ex_in_specs, ex_out_specs, ex_out_shape, ex_scratch = _hosted(exchange)

    def body(*refs):
        q_ref, k_ref, v_ref, g_ref, t_ref, du_ref, dw_ref, dqd_ref, dkd_ref, da_ref, dgl_ref = refs[:11]
        dqkv_ref, dg_ref = refs[11 + len(ex_in):13 + len(ex_in)]
        ex_refs = refs[11:11 + len(ex_in)] + refs[13 + len(ex_in):]
        h = pl.program_id(1)

        if exchange is not None:
            @pl.when((pl.program_id(0) == 0) & (h == 0))
            def _():
                exchange.start(*exchange.split(ex_refs))

        @pl.when(h == 0)
        def _():
            dg_ref[...] = jnp.zeros_like(dg_ref)

        lane = lax.broadcasted_iota(jnp.int32, (CHUNK, LANES), 1)
        chunks = [pl.ds(cidx * CHUNK, CHUNK) for cidx in range(CB)]
        gts = [g_ref[rows, :] for rows in chunks]
        t_saved = [t_ref[rows, :] for rows in chunks]
        _, vjp = jax.vjp(lambda *args: [o[:6] for o in _chunk_prep(*args, t_saved=t_saved)],
                         [q_ref[rows, :] for rows in chunks], [k_ref[rows, :] for rows in chunks],
                         [v_ref[rows, :] for rows in chunks], [_lane_col(gt, LANE_G + h) for gt in gts],
                         [_lane_col(gt, LANE_BETA + h) for gt in gts])
        dqs, dks, dvs, dgcs, dbcs = vjp([(du_ref[rows, :], dw_ref[rows, :], da_ref[rows, :], dqd_ref[rows, :],
                                          dkd_ref[rows, :], dgl_ref[cidx][:, 0:1]) for cidx, rows in enumerate(chunks)])
        for cidx, rows in enumerate(chunks):
            dq, dk, dv, dgc, dbc = dqs[cidx], dks[cidx], dvs[cidx], dgcs[cidx], dbcs[cidx]
            dqkv_ref[0, rows, :] = dq
            dqkv_ref[1, rows, :] = dk
            dqkv_ref[2, rows, :] = dv
            dg_ref[rows, :] += jnp.where(lane == LANE_G + h, dgc, 0.0) + jnp.where(lane == LANE_BETA + h, dbc, 0.0)

        if exchange is not None:
            @pl.when((pl.program_id(0) == NCB - 1) & (h == NGH - 1))
            def _():
                exchange.finish(*exchange.split(ex_refs))

    qs, ks, vs, gs, a_s, gl_s = _prep_specs()
    res = pl.pallas_call(
        body, name="gdn_prep_bwd", grid=(NCB, NGH), in_specs=[qs, ks, vs, gs, a_s, qs, qs, qs, qs, a_s, gl_s] + ex_in_specs,
        out_specs=[pl.BlockSpec((3, CB * CHUNK, GHD), lambda i, h: (0, i, h)), gs] + ex_out_specs,
        out_shape=[jax.ShapeDtypeStruct((3, S, DGDN), F32), jax.ShapeDtypeStruct((S, LANES), F32)] + ex_out_shape,
        scratch_shapes=ex_scratch, compiler_params=_cparams(),
    )(qkv, qkv, qkv, gates, t_inv, du, dw, dqd, dkd, da, dgl, *ex_in)
    return res[0], res[1], res[2:]


def _scan_specs(nh, parts, reverse):
    wide, rows, chunks = nh * GHD, S // parts, NCH // parts

    def part(p):
        return parts - 1 - p if reverse else p

    hs = pl.BlockSpec((rows, wide), lambda g, p: (part(p), g))
    a_s = pl.BlockSpec((nh, rows, CHUNK), lambda g, p: (g, part(p), 0))
    gl_s = pl.BlockSpec((nh, chunks, 1, LANES), lambda g, p: (g, part(p), 0, 0))
    st_s = pl.BlockSpec((nh, chunks, GHD, GHD), lambda g, p: (g, part(p), 0, 0))
    gz_s = pl.BlockSpec((rows, wide), lambda g, p: (part(p), BLK_GZ // nh + g))
    mix_s = pl.BlockSpec((rows, wide), lambda g, p: (part(p), NPAIR // nh + g))
    return hs, a_s, gl_s, st_s, gz_s, mix_s


def _head_cols(hh):
    return slice(hh * GHD, (hh + 1) * GHD)


SCAN_HEADS, SCAN_PARTS = 4, 2
SCAN_HEADS_BWD, SCAN_PARTS_BWD = 2, 2


def _gdn_scan(u, w, qd, kd, a, gl, proj, w_norm, mix):
    heads = range(SCAN_HEADS)

    def body(u_ref, w_ref, qd_ref, kd_ref, a_ref, gl_ref, z_ref, wn_ref, mix_in, mix_ref, o_ref, st_ref, carry_ref):
        del mix_in

        @pl.when(pl.program_id(1) == 0)
        def _():
            carry_ref[...] = jnp.zeros_like(carry_ref)

        def step(ci, states):
            rows = pl.ds(pl.multiple_of(ci * CHUNK, CHUNK), CHUNK)
            for hh in heads:
                st_ref[hh, ci] = states[hh]
            ws = [_dot(w_ref[rows, _head_cols(hh)], states[hh]) for hh in heads]
            qs = [_dot(qd_ref[rows, _head_cols(hh)], states[hh]) for hh in heads]
            vn = [u_ref[rows, _head_cols(hh)] - ws[hh] for hh in heads]
            av = [_dot(a_ref[hh, rows, :], vn[hh]) for hh in heads]
            kv = [_dot(kd_ref[rows, _head_cols(hh)], vn[hh], 0, 0) for hh in heads]
            for hh in heads:
                o_ref[rows, _head_cols(hh)] = qs[hh] + av[hh]
            return tuple(states[hh] * gl_ref[hh, ci] + kv[hh] for hh in heads)

        last = lax.fori_loop(0, NCH // SCAN_PARTS, step, tuple(carry_ref[hh] for hh in heads))
        for hh in heads:
            carry_ref[hh] = last[hh]
            ov = o_ref[:, _head_cols(hh)]
            mix_ref[:, _head_cols(hh)] = (ov * _rms_scale(ov) * wn_ref[...] * _silu(z_ref[:, _head_cols(hh)])).astype(BF16)

    hs, a_s, gl_s, st_s, gz_s, mix_s = _scan_specs(SCAN_HEADS, SCAN_PARTS, False)
    return pl.pallas_call(
        body, name="gdn_scan", grid=(NGH // SCAN_HEADS, SCAN_PARTS),
        in_specs=[hs, hs, hs, hs, a_s, gl_s, gz_s, pl.BlockSpec((1, GHD), lambda g, p: (0, 0)),
                  pl.BlockSpec(memory_space=pl.ANY)],
        out_specs=[mix_s, hs, st_s],
        out_shape=[jax.ShapeDtypeStruct((S, D), BF16), jax.ShapeDtypeStruct((S, DGDN), F32),
                   jax.ShapeDtypeStruct((NGH, NCH, GHD, GHD), F32)],
        input_output_aliases={8: 0}, scratch_shapes=[pltpu.VMEM((SCAN_HEADS, GHD, GHD), F32)], compiler_params=_cparams(),
    )(u, w, qd, kd, a, gl, proj, w_norm, mix)


def _gdn_scan_bwd(dmix, o, proj, w_norm, u, w, qd, kd, a, gl, states, dproj, exchange=None):
    ex_in, ex_in_specs, ex_out_specs, ex_out_shape, ex_scratch = _hosted(exchange)
    groups = NGH // SCAN_HEADS_BWD

    def body(*refs):
        dy_ref, o_ref, z_ref, wn_ref, u_ref, w_ref, qd_ref, kd_ref, a_ref, gl_ref, st_ref = refs[:11]
        dz_ref, du_ref, dw_ref, dqd_ref, dkd_ref, da_ref, dgl_ref, dwn_ref = refs[12 + len(ex_in):20 + len(ex_in)]
        do_ref, carry_ref = refs[20 + len(ex_in) + len(ex_out_shape):22 + len(ex_in) + len(ex_out_shape)]
        ex_refs = refs[12:12 + len(ex_in)] + refs[20 + len(ex_in):20 + len(ex_in) + len(ex_out_shape)] + refs[-2:]
        heads = range(SCAN_HEADS_BWD)
        chunks = NCH // SCAN_PARTS_BWD

        if exchange is not None:
            @pl.when((pl.program_id(0) == 0) & (pl.program_id(1) == 0))
            def _():
                exchange.start(*exchange.split(ex_refs))

        @pl.when((pl.program_id(0) == 0) & (pl.program_id(1) == 0))
        def _():
            dwn_ref[...] = jnp.zeros_like(dwn_ref)

        @pl.when(pl.program_id(1) == 0)
        def _():
            carry_ref[...] = jnp.zeros_like(carry_ref)

        wn = wn_ref[...]
        for hh in heads:
            c = _head_cols(hh)
            ov = o_ref[:, c]
            zv = z_ref[:, c]
            g = dy_ref[:, c]
            sig = _sigmoid(zv)
            dz_ref[:, c] = (g * (ov * _rms_scale(ov) * wn) * sig * (1.0 + zv * (1.0 - sig))).astype(BF16)
            do, dwt = _rms_bwd(ov, wn, g * zv * sig)
            do_ref[:, c] = do
            dwn_ref[...] += jnp.sum(dwt, axis=0, keepdims=True)

        def step(t, dstates):
            ci = chunks - 1 - t
            rows = pl.ds(pl.multiple_of(ci * CHUNK, CHUNK), CHUNK)
            cols = [_head_cols(hh) for hh in heads]
            state = [st_ref[hh, ci] for hh in heads]
            dov = [do_ref[rows, cols[hh]] for hh in heads]
            wv = [w_ref[rows, cols[hh]] for hh in heads]
            ws = [_dot(wv[hh], state[hh]) for hh in heads]
            adov = [_dot(a_ref[hh, rows, :], dov[hh], 0, 0) for hh in heads]
            kds = [_dot(kd_ref[rows, cols[hh]], dstates[hh]) for hh in heads]
            dqd = [_dot(dov[hh], state[hh], 1, 1) for hh in heads]
            qdo = [_dot(qd_ref[rows, cols[hh]], dov[hh], 0, 0) for hh in heads]
            vn = [u_ref[rows, cols[hh]] - ws[hh] for hh in heads]
            dvn = [adov[hh] + kds[hh] for hh in heads]
            da = [_dot(dov[hh], vn[hh], 1, 1) for hh in heads]
            dkd = [_dot(vn[hh], dstates[hh], 1, 1) for hh in heads]
            dwv = [_dot(dvn[hh], state[hh], 1, 1) for hh in heads]
            wdv = [_dot(wv[hh], dvn[hh], 0, 0) for hh in heads]
            for hh in heads:
                da_ref[hh, rows, :] = da[hh]
                dqd_ref[rows, cols[hh]] = dqd[hh]
                dkd_ref[rows, cols[hh]] = dkd[hh]
                dgl = jnp.sum(jnp.sum(dstates[hh] * state[hh], axis=1, keepdims=True), axis=0, keepdims=True)
                dgl_ref[hh, ci] = jnp.broadcast_to(dgl, (1, LANES))
                du_ref[rows, cols[hh]] = dvn[hh]
                dw_ref[rows, cols[hh]] = -dwv[hh]
            return tuple(dstates[hh] * gl_ref[hh, ci] + qdo[hh] - wdv[hh] for hh in heads)

        last = lax.fori_loop(0, chunks, step, tuple(carry_ref[hh] for hh in heads))
        for hh in heads:
            carry_ref[hh] = last[hh]

        if exchange is not None:
            @pl.when((pl.program_id(0) == groups - 1) & (pl.program_id(1) == SCAN_PARTS_BWD - 1))
            def _():
                exchange.finish(*exchange.split(ex_refs))

    hs, a_s, gl_s, st_s, gz_s, mix_s = _scan_specs(SCAN_HEADS_BWD, SCAN_PARTS_BWD, True)
    vec = pl.BlockSpec((1, GHD), lambda g, p: (0, 0))
    tok = jax.ShapeDtypeStruct((S, DGDN), F32)
    res = pl.pallas_call(
        body, name="gdn_scan_bwd", grid=(groups, SCAN_PARTS_BWD),
        in_specs=[mix_s, hs, gz_s, vec, hs, hs, hs, hs, a_s, gl_s, st_s, pl.BlockSpec(memory_space=pl.ANY)] + ex_in_specs,
        out_specs=[gz_s, hs, hs, hs, hs, a_s, gl_s, vec] + ex_out_specs,
        out_shape=[jax.ShapeDtypeStruct((S, DPROJ_PAD), BF16), tok, tok, tok, tok,
                   jax.ShapeDtypeStruct((NGH, S, CHUNK), F32), jax.ShapeDtypeStruct((NGH, NCH, 1, LANES), F32),
                   jax.ShapeDtypeStruct((1, GHD), F32)] + ex_out_shape,
        input_output_aliases={11: 0},
        scratch_shapes=[pltpu.VMEM((S // SCAN_PARTS_BWD, SCAN_HEADS_BWD * GHD), F32),
                        pltpu.VMEM((SCAN_HEADS_BWD, GHD, GHD), F32)] + ex_scratch,
        compiler_params=_cparams(),
    )(dmix, o, proj, w_norm, u, w, qd, kd, a, gl, states, dproj, *ex_in)
    return res[:8], res[8:]


def _place():
    return lax.axis_index("x"), lax.axis_index("y"), lax.axis_index("c")


def _other_chips(x, y):
    return [(1 - x, y), (x, 1 - y), (1 - x, 1 - y)]


HBM = pl.BlockSpec(memory_space=pltpu.HBM)
VMEM = pl.BlockSpec(memory_space=pltpu.VMEM)


def _half_rows(ref_or_rows, half):
    rows = ref_or_rows // 2
    return pl.ds(pl.multiple_of(half * rows, rows), rows)


class _Exchange:
    def __init__(self, inputs, out_shape, n_sems, start, finish=None, middle=None, rest=None):
        self.inputs, self.out_shape, self.n_sems, self.start = inputs, out_shape, n_sems, start
        if finish is None:
            def finish(*refs):
                middle(*refs)
                rest(*refs)
        self.finish = finish
        self.middle = middle if middle is not None else (lambda *refs: None)
        self.rest = rest if rest is not None else finish

    def sem_shapes(self):
        return [pltpu.SemaphoreType.DMA((self.n_sems,)), pltpu.SemaphoreType.DMA((self.n_sems,))]

    def split(self, refs):
        n_in, n_out = len(self.inputs), len(self.out_shape)
        return refs[:n_in], refs[n_in:n_in + n_out], refs[n_in + n_out], refs[n_in + n_out + 1]


def _run_exchange(ex, name):
    def body(*refs):
        parts = ex.split(refs)
        ex.start(*parts)
        ex.finish(*parts)

    return pl.pallas_call(
        body, name=name, in_specs=[HBM] * len(ex.inputs), out_specs=[HBM] * len(ex.out_shape), out_shape=ex.out_shape,
        scratch_shapes=ex.sem_shapes(), compiler_params=_cparams(),
    )(*ex.inputs)


def _allgather_exchange(shards, whole=()):
    n, nw = len(shards), len(whole)
    slots = 8

    def plan(src, outs, send_sems, recv_sems):
        x, y, c = _place()
        via_x, via_y, diagonal = _other_chips(x, y)
        id_x, id_y, id_diagonal = [2 * chip[0] + chip[1] for chip in (via_x, via_y, diagonal)]
        me, sibling = (x, y, c), (x, y, 1 - c)

        def rows_of(a, half, quarter):
            total = src[a].shape[0]
            if quarter is None:
                return _half_rows(total, half)
            return pl.ds(pl.multiple_of(half * (total // 2) + quarter * (total // 4), total // 4), total // 4)

        def copy(a, k, chip_index, half, quarter, to, from_src=False):
            rows = rows_of(a, half, quarter)
            dst = outs[a].at[chip_index, rows]
            return pltpu.make_async_remote_copy(
                src_ref=src[a].at[rows] if from_src else dst, dst_ref=dst, send_sem=send_sems.at[slots * a + k],
                recv_sem=recv_sems.at[slots * a + k], device_id=to, device_id_type=MESH)

        def whole_copy(b, k, chip_index, to):
            return pltpu.make_async_remote_copy(
                src_ref=src[n + b], dst_ref=outs[n + b].at[chip_index], send_sem=send_sems.at[slots * n + 3 * b + k],
                recv_sem=recv_sems.at[slots * n + 3 * b + k], device_id=to, device_id_type=MESH)

        first, stages, last = [], [], []
        for a in range(n):
            first += [copy(a, 0, 2 * x + y, c, None, (*via_x, c), True), copy(a, 1, 2 * x + y, c, None, (*via_y, c), True)]
            stages.append([
                (copy(a, 0, id_x, c, None, me),
                 [copy(a, 2, id_x, c, 0, (*via_y, c)), copy(a, 4, id_x, c, None, sibling)]),
                (copy(a, 1, id_y, c, None, me),
                 [copy(a, 3, id_y, c, 1, (*via_x, c)), copy(a, 5, id_y, c, None, sibling)]),
                (copy(a, 2, id_diagonal, c, 0, me), [copy(a, 6, id_diagonal, c, 0, sibling)]),
                (copy(a, 3, id_diagonal, c, 1, me), [copy(a, 7, id_diagonal, c, 1, sibling)]),
            ])
            last += [copy(a, 4, id_x, 1 - c, None, me), copy(a, 5, id_y, 1 - c, None, me),
                     copy(a, 6, id_diagonal, 1 - c, 0, me), copy(a, 7, id_diagonal, 1 - c, 1, me)]
        for b in range(nw):
            for k, (chip, index) in enumerate(((via_x, id_x), (via_y, id_y), (diagonal, id_diagonal))):
                first.append(whole_copy(b, k, 2 * x + y, (*chip, c)))
                last.append(whole_copy(b, k, index, me))
        return first, stages, last

    def start(*refs):
        for cp in plan(*refs)[0]:
            cp.start()

    def pass_on(stages, which):
        for stage in which:
            for per_shard in stages:
                lands, onward = per_shard[stage]
                lands.wait_recv()
                for cp in onward:
                    cp.start()

    def middle(*refs):
        pass_on(plan(*refs)[1], (0, 1))

    def rest(*refs):
        first, stages, last = plan(*refs)
        pass_on(stages, (2, 3))
        for cp in last:
            cp.wait_recv()
        for cp in first + [cp for per_shard in stages for _, onward in per_shard for cp in onward]:
            cp.wait_send()

    out_shape = [jax.ShapeDtypeStruct((NCHIP,) + s.shape, s.dtype) for s in list(shards) + list(whole)]
    return _Exchange(list(shards) + list(whole), out_shape, slots * n + 3 * nw, start, middle=middle, rest=rest)


def _with_own(gathered, own):
    x, y, _ = _place()
    return lax.dynamic_update_index_in_dim(gathered, own, 2 * x + y, axis=0)


def _simple_exchange(inputs, out_shape, copies_of):
    def start(*refs):
        for cp in copies_of(*refs):
            cp.start()

    def finish(*refs):
        for cp in copies_of(*refs):
            cp.wait()

    return _Exchange(list(inputs), out_shape, len(out_shape) * 3, start, finish)


def _pair_exchange(grads):
    def copies_of(src, outs, send_sems, recv_sems):
        x, y, c = _place()
        return [pltpu.make_async_remote_copy(
            src_ref=src[a].at[:, _half_rows(src[a].shape[1], 1 - c)], dst_ref=outs[a], send_sem=send_sems.at[a],
            recv_sem=recv_sems.at[a], device_id=(x, y, 1 - c), device_id_type=MESH) for a in range(len(src))]

    return _simple_exchange(
        grads, [jax.ShapeDtypeStruct((g.shape[0], g.shape[1] // 2, g.shape[2]), g.dtype) for g in grads], copies_of)


def _pair_sum(grads, theirs, name):
    n = len(grads)

    def body(*refs):
        south = lax.axis_index("c") == 0
        for a in range(n):
            g = refs[a][...]
            half = g.shape[0] // 2
            mine = jnp.where(south, g[:half], g[half:])
            refs[2 * n + a][...] = (mine.astype(F32) + refs[n + a][...].astype(F32)).astype(BF16)

    def specs(arrs):
        return [pl.BlockSpec((None,) + g.shape[1:], lambda j: (j, 0, 0)) for g in arrs]

    return pl.pallas_call(
        body, name=name, grid=(NCHIP,), in_specs=specs(grads) + specs(theirs), out_specs=specs(theirs),
        out_shape=[jax.ShapeDtypeStruct(g.shape, BF16) for g in theirs], compiler_params=_cparams(),
    )(*grads, *theirs)


def _chip_exchange(parts):
    n = len(parts)
    slots = 6

    def plan(src, outs, send_sems, recv_sems):
        x, y, c = _place()
        via_x, via_y, diagonal = _other_chips(x, y)
        id_x, id_y, id_diagonal = [2 * chip[0] + chip[1] for chip in (via_x, via_y, diagonal)]
        first, relays, last = [], [], []
        for a in range(n):
            received, passing = outs[a], outs[n + a]
            half = src[a].shape[1] // 2

            def copy(k, source, target, to, a=a):
                return pltpu.make_async_remote_copy(
                    src_ref=source, dst_ref=target, send_sem=send_sems.at[slots * a + k],
                    recv_sem=recv_sems.at[slots * a + k], device_id=(*to, c), device_id_type=MESH)

            first += [copy(0, src[a].at[id_x], received.at[0], via_x), copy(1, src[a].at[id_y], received.at[1], via_y),
                      copy(2, src[a].at[id_diagonal, 0:half], passing.at[0], via_x),
                      copy(3, src[a].at[id_diagonal, half:2 * half], passing.at[1], via_y)]
            relays += [(copy(2, src[a].at[id_diagonal, 0:half], passing.at[0], via_x),
                        copy(4, passing.at[0], received.at[2, 0:half], via_y)),
                       (copy(3, src[a].at[id_diagonal, half:2 * half], passing.at[1], via_y),
                        copy(5, passing.at[1], received.at[2, half:2 * half], via_x))]
            last += [copy(0, src[a].at[id_x], received.at[0], via_x), copy(1, src[a].at[id_y], received.at[1], via_y),
                     copy(4, passing.at[0], received.at[2, 0:half], via_y),
                     copy(5, passing.at[1], received.at[2, half:2 * half], via_x)]
        return first, relays, last

    def start(*refs):
        for cp in plan(*refs)[0]:
            cp.start()

    def middle(*refs):
        for lands, onward in plan(*refs)[1]:
            lands.wait_recv()
            onward.start()

    def rest(*refs):
        first, relays, last = plan(*refs)
        for cp in last:
            cp.wait_recv()
        for cp in first + [onward for _, onward in relays]:
            cp.wait_send()

    out_shape = [jax.ShapeDtypeStruct((NCHIP - 1,) + p.shape[1:], p.dtype) for p in parts]
    out_shape += [jax.ShapeDtypeStruct((2, p.shape[1] // 2, p.shape[2]), p.dtype) for p in parts]
    return _Exchange(list(parts), out_shape, slots * n, start, middle=middle, rest=rest)


def _chip_sum(parts, received, exchange=None):
    n = len(parts)
    steps = 4
    ex_in, ex_in_specs, ex_out_specs, ex_out_shape, ex_scratch = _hosted(exchange)

    def body(*refs):
        ex_refs = refs[2 * n:2 * n + len(ex_in)] + refs[3 * n + len(ex_in):]
        if exchange is not None:
            @pl.when(pl.program_id(0) == 0)
            def _():
                exchange.start(*exchange.split(ex_refs))

        chip = 2 * lax.axis_index("x") + lax.axis_index("y")
        for a in range(n):
            p, r = refs[a], refs[n + a]
            own = jnp.where(chip == 0, p[0], jnp.where(chip == 1, p[1], jnp.where(chip == 2, p[2], p[3])))
            refs[2 * n + len(ex_in) + a][...] = ((own.astype(F32) + r[0].astype(F32)) + r[1].astype(F32)) + r[2].astype(F32)

        if exchange is not None:
            @pl.when(pl.program_id(0) == steps - 1)
            def _():
                exchange.finish(*exchange.split(ex_refs))

    def specs(arrs):
        return [pl.BlockSpec((g.shape[0], g.shape[1] // steps, g.shape[2]), lambda i: (0, i, 0)) for g in arrs]

    out_specs = [pl.BlockSpec((g.shape[1] // steps, g.shape[2]), lambda i: (i, 0)) for g in parts]
    res = pl.pallas_call(
        body, name="grads_chip_sum", grid=(steps,), in_specs=specs(parts) + specs(received) + ex_in_specs,
        out_specs=out_specs + ex_out_specs, out_shape=[jax.ShapeDtypeStruct(g.shape[1:], F32) for g in parts] + ex_out_shape,
        scratch_shapes=ex_scratch, compiler_params=_cparams(),
    )(*parts, *received, *ex_in)
    return res[:n], res[n:]


def _pair_share(halves):
    def copies_of(src, outs, send_sems, recv_sems):
        x, y, c = _place()
        return [pltpu.make_async_remote_copy(
            src_ref=src[a], dst_ref=outs[a], send_sem=send_sems.at[a], recv_sem=recv_sems.at[a],
            device_id=(x, y, 1 - c), device_id_type=MESH) for a in range(len(src))]

    return _simple_exchange(halves, [jax.ShapeDtypeStruct(h.shape, F32) for h in halves], copies_of)


def _adamw_math(w, g, m, v):
    nm = ADAM_B1 * m + (1.0 - ADAM_B1) * g
    nv = ADAM_B2 * v + (1.0 - ADAM_B2) * jnp.square(g)
    m_hat = nm / (1.0 - ADAM_B1 ** ADAM_STEP)
    v_hat = nv / (1.0 - ADAM_B2 ** ADAM_STEP)
    return -ADAM_LR * (m_hat / (jnp.sqrt(v_hat) + ADAM_EPS) + ADAM_WD * w), nm, nv


def _adamw_big(ws, g_mine, g_theirs, ms, vs, exchange=None):
    n = len(ws)
    steps = 8
    ex_in, ex_in_specs, ex_out_specs, ex_out_shape, ex_scratch = _hosted(exchange)

    def body(*refs):
        ex_refs = refs[5 * n:5 * n + len(ex_in)] + refs[9 * n + len(ex_in):]
        outs = refs[5 * n + len(ex_in):9 * n + len(ex_in)]
        if exchange is not None:
            @pl.when(pl.program_id(0) == 0)
            def _():
                exchange.start(*exchange.split(ex_refs))

        own_half = (pl.program_id(0) // (steps // 2)) == lax.axis_index("c")
        for a in range(n):
            g = jnp.where(own_half, refs[n + a][...], refs[2 * n + a][...])
            d, nm, nv = _adamw_math(refs[a][...], g, refs[3 * n + a][...], refs[4 * n + a][...])
            outs[a][...] = g
            outs[n + a][...] = d
            outs[2 * n + a][...] = nm
            outs[3 * n + a][...] = nv

        if exchange is not None:
            @pl.when(pl.program_id(0) == steps - 1)
            def _():
                exchange.finish(*exchange.split(ex_refs))

    specs = [pl.BlockSpec((w.shape[0] // steps, w.shape[1]), lambda i: (i, 0)) for w in ws]
    half_specs = [pl.BlockSpec((g.shape[0] // (steps // 2), g.shape[1]), lambda i: (i % (steps // 2), 0)) for g in g_mine]
    shapes = [jax.ShapeDtypeStruct(w.shape, F32) for w in ws]
    res = pl.pallas_call(
        body, name="adamw_big", grid=(steps,), in_specs=specs + half_specs * 2 + specs * 2 + ex_in_specs,
        out_specs=specs * 4 + ex_out_specs, out_shape=shapes * 4 + ex_out_shape, scratch_shapes=ex_scratch,
        compiler_params=_cparams(),
    )(*ws, *g_mine, *g_theirs, *ms, *vs, *ex_in)
    return res[:n], res[n:2 * n], res[2 * n:3 * n], res[3 * n:4 * n], res[4 * n:]


def _adamw_in(w, g_mine, g_theirs, m, v):
    half = D // 2

    def body(w_ref, gm_ref, gt_ref, m_ref, v_ref, g_out, d_out, nm_out, nv_out, g_ref):
        south = lax.axis_index("c") == 0
        g_ref[0:half, :] = jnp.where(south, gm_ref[...], gt_ref[...])
        g_ref[half:D, :] = jnp.where(south, gt_ref[...], gm_ref[...])
        g = g_ref[0:CW, :]
        d, nm, nv = _adamw_math(w_ref[...], g, m_ref[...], v_ref[...])
        g_out[...] = g
        d_out[...] = d
        nm_out[...] = nm
        nv_out[...] = nv

    spec = pl.BlockSpec((CW, LANES), lambda i: (0, i))
    half_spec = pl.BlockSpec((half, LANES), lambda i: (0, i))
    return pl.pallas_call(
        body, name="adamw_in", grid=(D // LANES,), in_specs=[spec, half_spec, half_spec, spec, spec], out_specs=[spec] * 4,
        out_shape=[jax.ShapeDtypeStruct((CW, D), F32)] * 4, scratch_shapes=[pltpu.VMEM((D, LANES), F32)],
        compiler_params=_cparams(),
    )(w, g_mine, g_theirs, m, v)


NORM_NAMES = ("pre_mix_norm", "post_mix_norm", "pre_mlp_norm", "post_mlp_norm")
SMALL_NAMES = NORM_NAMES + ("gdn_conv_w", "fox_f_bias", "gdn_dt_bias", "gdn_a_log", "fox_out_norm", "gdn_out_norm")
CONV_COLS = 3 * DGDN // NCHIP


def _small_gather(d_norms, d_conv, sums, d_fox_norm, d_gdn_norm, loss_row):
    n_arrays = 6
    n_remote = n_arrays * (NDEV - 1)

    def copies_of(src, outs, send_sems, recv_sems):
        x, y, c = _place()
        me = 4 * x + 2 * y + c

        def from_me(chip_index):
            cols = pl.ds(pl.multiple_of(chip_index * CONV_COLS, LANES), CONV_COLS)
            return [src[0], src[1].at[:, cols], src[2], src[3], src[4], src[5]]

        local = [pltpu.make_async_copy(s, outs[a].at[me], send_sems.at[n_remote + a]) for a, s in enumerate(from_me(2 * x + y))]
        remote = []
        for k in range(1, NDEV):
            px, py, pc = x ^ ((k >> 2) & 1), y ^ ((k >> 1) & 1), c ^ (k & 1)
            remote += [pltpu.make_async_remote_copy(
                src_ref=s, dst_ref=outs[a].at[me], send_sem=send_sems.at[n_arrays * (k - 1) + a],
                recv_sem=recv_sems.at[n_arrays * (k - 1) + a], device_id=(px, py, pc), device_id_type=MESH)
                for a, s in enumerate(from_me(2 * px + py))]
        return local + remote

    def start(*refs):
        for cp in copies_of(*refs):
            cp.start()

    def finish(*refs):
        for cp in copies_of(*refs):
            cp.wait()

    shapes = [(4, D), (CONV_K, CONV_COLS), (8, LANES), (1, LANES), (1, LANES), (1, LANES)]
    return _Exchange([d_norms, d_conv, sums, d_fox_norm, d_gdn_norm, loss_row],
                     [jax.ShapeDtypeStruct((NDEV,) + s, F32) for s in shapes], n_remote + n_arrays, start, finish)


def _small_adamw(gathered, ws, ms, vs):
    n = len(SMALL_NAMES)
    ng = len(gathered)

    def body(*refs):
        def total(buf):
            acc = buf[0]
            for i in range(1, NDEV):
                acc = acc + buf[i]
            return acc

        t_norms, t_conv, t_sums, t_fn, t_gn, t_loss = [total(r) for r in refs[:ng]]
        w_refs, m_refs, v_refs = refs[ng:ng + n], refs[ng + n:ng + 2 * n], refs[ng + 2 * n:ng + 3 * n]
        outs = refs[ng + 3 * n:]
        outs[4 * n][...] = t_loss
        grads = [t_norms[i:i + 1, :] for i in range(4)] + [
            t_conv, t_sums[0:1, 0:NFH], t_sums[1:2, 0:NGH], t_sums[2:3, 0:NGH], t_fn[:, 0:FHD], t_gn]
        for a in range(n):
            d, nm, nv = _adamw_math(w_refs[a][...], grads[a], m_refs[a][...], v_refs[a][...])
            outs[a][...] = grads[a]
            outs[n + a][...] = d
            outs[2 * n + a][...] = nm
            outs[3 * n + a][...] = nv

    def whole(arr):
        return pl.BlockSpec(arr.shape, lambda i: (0,) * arr.ndim)

    res = pl.pallas_call(
        body, name="small_adamw", grid=(1,), in_specs=[whole(t) for t in gathered] + [whole(w) for w in ws] * 3,
        out_specs=[whole(w) for w in ws] * 4 + [pl.BlockSpec((1, LANES), lambda i: (0, 0))],
        out_shape=[jax.ShapeDtypeStruct(w.shape, F32) for w in ws] * 4 + [jax.ShapeDtypeStruct((1, LANES), F32)],
        compiler_params=_cparams(),
    )(*gathered, *ws, *ms, *vs)
    return res[:n], res[n:2 * n], res[2 * n:3 * n], res[3 * n:4 * n], res[4 * n]


CW = DPROJ // NCHIP
PROJ_RUNS = tuple((part * DFOX + hp * LANES, part * DFOX + (hp + 1) * LANES, (3 * hp + part) * LANES)
                  for hp in range(NPAIR) for part in range(3)) + (
    (1536, 1544, BLK_SMALL * LANES), (1544, 3080, BLK_GDN * LANES), (3080, 3088, BLK_SMALL * LANES + 8),
    (3088, 3600, BLK_GZ * LANES))


def _proj_pieces():
    pieces = []
    for lo, hi, at in PROJ_RUNS:
        while lo < hi:
            j = lo // CW
            end = min(hi, (j + 1) * CW)
            pieces.append((j, lo - j * CW, at, end - lo))
            at, lo = at + end - lo, end
    return pieces


RT = 256


def _to_padded_rows(gathered):
    def body(src_ref, out_ref, blocks_ref, rows_ref):
        blocks_ref[...] = src_ref[...].astype(F32)
        rows_ref[...] = jnp.zeros_like(rows_ref)
        for j, start, at, n in _proj_pieces():
            rows_ref[at:at + n, :] = blocks_ref[j, start:start + n, :]
        out_ref[...] = rows_ref[...].astype(out_ref.dtype)

    return pl.pallas_call(
        body, name="proj_rows_in", grid=(D // RT,), in_specs=[pl.BlockSpec((NCHIP, D, RT), lambda i: (0, 0, i))],
        out_specs=pl.BlockSpec((DPROJ_PAD, RT), lambda i: (0, i)), out_shape=jax.ShapeDtypeStruct((DPROJ_PAD, D), gathered.dtype),
        scratch_shapes=[pltpu.VMEM((NCHIP, D, RT), F32), pltpu.VMEM((DPROJ_PAD, RT), F32)], compiler_params=_cparams(),
    )(gathered)


def _from_padded_rows(w):
    def body(src_ref, out_ref, rows_ref, blocks_ref):
        rows_ref[...] = src_ref[...].astype(F32)
        blocks_ref[...] = jnp.zeros_like(blocks_ref)
        for j, start, at, n in _proj_pieces():
            blocks_ref[j, start:start + n, :] = rows_ref[at:at + n, :]
        out_ref[...] = blocks_ref[...].astype(out_ref.dtype)

    return pl.pallas_call(
        body, name="proj_rows_out", grid=(D // RT,), in_specs=[pl.BlockSpec((DPROJ_PAD, RT), lambda i: (0, i))],
        out_specs=pl.BlockSpec((NCHIP, D, RT), lambda i: (0, 0, i)), out_shape=jax.ShapeDtypeStruct((NCHIP, D, D), w.dtype),
        scratch_shapes=[pltpu.VMEM((DPROJ_PAD, RT), F32), pltpu.VMEM((NCHIP, D, RT), F32)], compiler_params=_cparams(),
    )(w)


def _local_step(x, target, first_weights, late_weights, reduce_late, reduce_in, pre_mix_norm, fox_f_bias, fox_out_norm,
                gdn_a_log, gdn_dt_bias, gdn_out_norm, post_mix_norm, pre_mlp_norm, post_mlp_norm):
    bias_vec = jnp.zeros((1, LANES), F32).at[0, 0:NFH].set(fox_f_bias).at[0, LANE_G:LANE_G + NGH].set(gdn_dt_bias)
    alog_vec = jnp.zeros((1, LANES), F32).at[0, LANE_G:LANE_G + NGH].set(gdn_a_log)
    w2 = jnp.concatenate([fox_out_norm, fox_out_norm], axis=1)

    h, first = _pre_norm(x, pre_mix_norm, exchange=first_weights[0])
    win_p, conv_w = first_weights[1](first)
    proj = _matmul(h, win_p, tb=True, tm=2048, tn=768, tk=1024, name="mm_proj", exchange=late_weights[0])
    proj, late_a = proj if late_weights[0] is not None else (proj, [])
    gates = _gates(proj, bias_vec, alog_vec)
    mix, fox_o, lse, late_b = _fox_fwd(proj, gates, w2, exchange=late_weights[1])
    qkv = _gdn_pre(proj, conv_w)
    (u, w, qd, kd, a_intra, gl, t_inv), late_c = _gdn_prep(qkv, gates, exchange=late_weights[2])
    wout, wup3, wdown = late_weights[3](late_a, late_b, late_c)
    mix, gdn_raw, states = _gdn_scan(u, w, qd, kd, a_intra, gl, proj, gdn_out_norm, mix)
    mixed = _matmul(mix, wout, tm=2048, tk=1024, name="mm_out")
    x1, h2 = _post_mix(x, mixed, post_mix_norm, pre_mlp_norm)

    def relu2(acc):
        r = jnp.maximum(acc, 0.0)
        return r, r * r

    up_relu, act = _matmul(h2, wup3, b3=True, tm=1024, tn=1024, tk=1024, out_dtypes=(BF16, BF16), epilogue=relu2,
                           name="mm_up")
    y = _matmul(act, wdown, tm=1024, tk=DFF, name="mm_down")
    dx2, dy, d_post_mlp, loss_row = _loss_head(x1, y, post_mlp_norm, target)

    dwdown = _matmul(act, dy, ta=True, tm=1024, tn=1024, tk=2048, out_dtypes=(BF16,), name="mm_dwdown")

    def relu2_bwd(acc, r):
        return (acc * 2.0 * r.astype(F32),)

    dup = _matmul(dy, wdown, tb=True, tm=1024, tn=1024, tk=1024, out_dtypes=(BF16,), extra=(up_relu,), epilogue=relu2_bwd,
                  name="mm_dact")
    dwup3 = _matmul(h2, dup, ta=True, tm=1024, tn=1024, tk=2048, out_dtypes=(BF16,), o3=True, name="mm_dwup")
    dh2 = _matmul(dup, wup3, tb=True, b3=True, tm=1024, tk=DFF, name="mm_dh2")
    dx1, dmixed, d_pre_mlp, d_post_mix = _mid_bwd(dh2, x1, pre_mlp_norm, dx2, mixed, post_mix_norm)
    dwout = _matmul(mix, dmixed, ta=True, tm=1024, tn=1024, tk=2048, out_dtypes=(BF16,), name="mm_dwout")
    dmix = _matmul(dmixed, wout, tb=True, tm=2048, tk=1024, name="mm_dmix")

    dfox, delta, d_fox_norm, from_sibling = _fox_norm_bwd(fox_o, dmix, w2, exchange=reduce_late[0](dwout, dwup3, dwdown))
    dproj, dcum_fox, reduced_a = _fox_bwd(proj, dfox, gates, lse, delta, exchange=reduce_late[1](from_sibling))
    (dproj, du, dw, dqd, dkd, da, dgl, d_gdn_norm), reduced_b = _gdn_scan_bwd(
        dmix, gdn_raw, proj, gdn_out_norm, u, w, qd, kd, a_intra, gl, states, dproj, exchange=reduce_late[2]())
    dqkv, dgates_gdn, reduced_c = _gdn_prep_bwd(qkv, gates, t_inv, du, dw, dqd, dkd, da, dgl, exchange=reduce_late[3]())
    reduced_late = (reduced_a, reduced_b, reduced_c)
    dproj, d_conv = _gdn_pre_bwd(proj, conv_w, dqkv, dproj)
    dproj, sums = _gates_bwd(proj, bias_vec, alog_vec, dgates_gdn, dcum_fox, dproj)

    dwin_p = _matmul(dproj, h, ta=True, tm=1280, tn=1024, tk=2048, out_dtypes=(BF16,), name="mm_dwin")
    exchange_in = reduce_in(dwin_p)
    dh = _matmul(dproj, win_p, tm=1024, tk=DPROJ_PAD, name="mm_dh", exchange=exchange_in)
    dh, reduced_in = dh if exchange_in is not None else (dh, [])
    grad_x, d_pre_mix = _pre_norm_bwd(dh, x, pre_mix_norm, dx1)

    d_norms = jnp.concatenate([d_pre_mix, d_post_mix, d_pre_mlp, d_post_mlp], axis=0)
    return grad_x, (d_norms, d_conv, sums, d_fox_norm, d_gdn_norm, loss_row), reduced_late, reduced_in


def kernel(x, pre_mix_norm, w_in, fox_f_bias, fox_out_norm, gdn_conv_w, gdn_a_log, gdn_dt_bias, gdn_out_norm, w_out, post_mix_norm, pre_mlp_norm, w_up, w_down, post_mlp_norm, loss_target, m_pre_mix_norm, m_w_in, m_fox_f_bias, m_fox_out_norm, m_gdn_conv_w, m_gdn_a_log, m_gdn_dt_bias, m_gdn_out_norm, m_w_out, m_post_mix_norm, m_pre_mlp_norm, m_w_up, m_w_down, m_post_mlp_norm, v_pre_mix_norm, v_w_in, v_fox_f_bias, v_fox_out_norm, v_gdn_conv_w, v_gdn_a_log, v_gdn_dt_bias, v_gdn_out_norm, v_w_out, v_post_mix_norm, v_pre_mlp_norm, v_w_up, v_w_down, v_post_mlp_norm):
    weights = dict(pre_mix_norm=pre_mix_norm, w_in=w_in, fox_f_bias=fox_f_bias, fox_out_norm=fox_out_norm, gdn_conv_w=gdn_conv_w,
                   gdn_a_log=gdn_a_log, gdn_dt_bias=gdn_dt_bias, gdn_out_norm=gdn_out_norm, w_out=w_out, post_mix_norm=post_mix_norm,
                   pre_mlp_norm=pre_mlp_norm, w_up=w_up, w_down=w_down, post_mlp_norm=post_mlp_norm)
    m_in = dict(pre_mix_norm=m_pre_mix_norm, w_in=m_w_in, fox_f_bias=m_fox_f_bias, fox_out_norm=m_fox_out_norm, gdn_conv_w=m_gdn_conv_w,
                gdn_a_log=m_gdn_a_log, gdn_dt_bias=m_gdn_dt_bias, gdn_out_norm=m_gdn_out_norm, w_out=m_w_out, post_mix_norm=m_post_mix_norm,
                pre_mlp_norm=m_pre_mlp_norm, w_up=m_w_up, w_down=m_w_down, post_mlp_norm=m_post_mlp_norm)
    v_in = dict(pre_mix_norm=v_pre_mix_norm, w_in=v_w_in, fox_f_bias=v_fox_f_bias, fox_out_norm=v_fox_out_norm, gdn_conv_w=v_gdn_conv_w,
                gdn_a_log=v_gdn_a_log, gdn_dt_bias=v_gdn_dt_bias, gdn_out_norm=v_gdn_out_norm, w_out=v_w_out, post_mix_norm=v_post_mix_norm,
                pre_mlp_norm=v_pre_mlp_norm, w_up=v_w_up, w_down=v_w_down, post_mlp_norm=v_post_mlp_norm)
    order_w = ("pre_mix_norm", "w_in", "fox_f_bias", "fox_out_norm", "gdn_conv_w", "gdn_a_log", "gdn_dt_bias", "gdn_out_norm", "w_out",
               "post_mix_norm", "pre_mlp_norm", "w_up", "w_down", "post_mlp_norm")
    big = ("w_in", "w_out", "w_up", "w_down")

    def row(v):
        return v if v.ndim == 2 else v.reshape(1, -1)

    win_shard = jnp.pad(w_in.T.astype(BF16), ((0, D - CW), (0, 0)))

    def resolve_first(gathered):
        win_g, conv_g = gathered
        return (_to_padded_rows(_with_own(win_g, win_shard)),
                _with_own(conv_g, gdn_conv_w).transpose(1, 0, 2).reshape(CONV_K, 3 * DGDN))

    late_shards = [weights[n].astype(BF16) for n in big[1:]]

    def resolve_late(gathered_out, gathered_mlp, _):
        wout_g, wup3, wdown_g = [_with_own(g, own) for g, own in zip(list(gathered_out) + list(gathered_mlp), late_shards)]
        return wout_g.reshape(D, D), wup3, wdown_g.reshape(DFF, D)

    pair_sums, late_blocks = {}, []

    def pair_summed(names, blocks, theirs):
        for n, s in zip(names, _pair_sum(blocks, theirs, "grads_pair_sum_" + names[0])):
            pair_sums[n] = s

    def late_pair_exchange(dwout, dwup3, dwdown):
        late_blocks.extend([dwout.reshape(NCHIP, D // NCHIP, D), dwup3, dwdown.reshape(NCHIP, DFF // NCHIP, D)])
        return _pair_exchange(late_blocks)

    def late_chip_exchange(theirs):
        pair_summed(big[1:], late_blocks, theirs)
        return _chip_exchange([pair_sums["w_up"], pair_sums["w_down"]])

    def reduce_in(dwin_p):
        blocks = [_from_padded_rows(dwin_p)]
        pair_summed(big[:1], blocks, _run_exchange(_pair_exchange(blocks), "grads_pair_exchange_w_in"))
        return _chip_exchange([pair_sums["w_in"]])

    grad_x, small, received_late, received_in = _local_step(
        x[0], loss_target[0], (_allgather_exchange([win_shard], whole=[gdn_conv_w]), resolve_first),
        (_allgather_exchange(late_shards[:1]), _allgather_exchange(late_shards[1:]), None, resolve_late),
        (late_pair_exchange, late_chip_exchange, lambda: None, lambda: _chip_exchange([pair_sums["w_out"]])),
        reduce_in, row(pre_mix_norm), fox_f_bias, row(fox_out_norm), gdn_a_log, gdn_dt_bias,
        row(gdn_out_norm), row(post_mix_norm), row(pre_mlp_norm), row(post_mlp_norm))
    received_mlp, _, received_out = received_late

    g_mine, small_gathered = _chip_sum(
        [pair_sums[n] for n in big], list(received_in[:1]) + list(received_out[:1]) + list(received_mlp[:2]),
        exchange=_small_gather(*small))
    g_theirs = _run_exchange(_pair_share(g_mine), "grads_pair_share")

    g_big, d_big, nm_big, nv_big, _ = _adamw_big(
        [weights[n] for n in big[1:]], g_mine[1:], g_theirs[1:], [m_in[n] for n in big[1:]], [v_in[n] for n in big[1:]])
    in_t = _adamw_in(w_in.T, g_mine[0], g_theirs[0], m_w_in.T, v_w_in.T)
    g_small, d_small, nm_small, nv_small, loss_total = _small_adamw(
        small_gathered, [row(weights[n]) for n in SMALL_NAMES], [row(m_in[n]) for n in SMALL_NAMES],
        [row(v_in[n]) for n in SMALL_NAMES])

    grads, delta, new_m, new_v = {}, {}, {}, {}
    grads["w_in"], delta["w_in"], new_m["w_in"], new_v["w_in"] = [t.T for t in in_t]
    for i, n in enumerate(big[1:]):
        grads[n], delta[n], new_m[n], new_v[n] = g_big[i], d_big[i], nm_big[i], nv_big[i]
    for i, n in enumerate(SMALL_NAMES):
        shape = weights[n].shape
        grads[n], delta[n], new_m[n], new_v[n] = (g_small[i].reshape(shape), d_small[i].reshape(shape),
                                                  nm_small[i].reshape(shape), nv_small[i].reshape(shape))
    return (loss_total[0, 0], grad_x[None], *[grads[n] for n in order_w], *[delta[n] for n in order_w], *[new_m[n] for n in order_w],
            *[new_v[n] for n in order_w])
```

```python
import jax
import jax.numpy as jnp
from jax import lax
from jax.experimental import pallas as pl
from jax.experimental.pallas import tpu as pltpu

F32 = jnp.float32
BF16 = jnp.bfloat16
MESH = pl.DeviceIdType.MESH

S = 2048
D = 1024
NFH, FHD = 8, 64
NPAIR = NFH // 2
NGH, GHD = 4, 128
DFOX = NFH * FHD
DGDN = NGH * GHD
CHUNK = 64
NCH = S // CHUNK
CONV_K = 4
DFF = 4 * D
EPS = 1e-6
DPROJ = 3600
LANES = 128
DPROJ_PAD = 3840
BLK_GDN = 12
BLK_GZ = 24
BLK_SMALL = 28
NCHIP = 4
NDEV = 8
VMEM_LIMIT = 56 * 1024 * 1024

ADAM_LR = 0.001
ADAM_B1 = 0.9
ADAM_B2 = 0.999
ADAM_EPS = 1e-08
ADAM_WD = 0.01
ADAM_STEP = 10


def _cparams(**kw):
    return pltpu.CompilerParams(vmem_limit_bytes=VMEM_LIMIT, **kw)


def _dn(ca, cb):
    return (((ca,), (cb,)), ((), ()))


def _dot(a, b, ca=1, cb=0):
    return lax.dot_general(a.astype(BF16), b.astype(BF16), _dn(ca, cb), preferred_element_type=F32)


def _hdot(a, b, ca=1, cb=0):
    return lax.dot_general(a.astype(F32), b.astype(F32), _dn(ca, cb), precision=lax.Precision.HIGHEST,
                           preferred_element_type=F32)


def _dot3(a, b, ca=1, cb=0):
    a_hi, b_hi = a.astype(BF16), b.astype(BF16)
    a_lo, b_lo = (a - a_hi.astype(F32)).astype(BF16), (b - b_hi.astype(F32)).astype(BF16)
    dn = _dn(ca, cb)
    return (lax.dot_general(a_hi, b_hi, dn, preferred_element_type=F32)
            + (lax.dot_general(a_hi, b_lo, dn, preferred_element_type=F32)
               + lax.dot_general(a_lo, b_hi, dn, preferred_element_type=F32)))


@jax.custom_vjp
def _mm_nn(a, b):
    return _dot(a, b, 1, 0)


def _mm_nn_fwd(a, b):
    return _dot(a, b, 1, 0), (a, b)


def _mm_nn_bwd(res, g):
    a, b = res
    return _dot(g, b, 1, 1), _dot(a, g, 0, 0)


_mm_nn.defvjp(_mm_nn_fwd, _mm_nn_bwd)


@jax.custom_vjp
def _mm_nt(a, b):
    return _dot(a, b, 1, 1)


def _mm_nt_fwd(a, b):
    return _dot(a, b, 1, 1), (a, b)


def _mm_nt_bwd(res, g):
    a, b = res
    return _dot(g, b, 1, 0), _dot(g, a, 0, 0)


_mm_nt.defvjp(_mm_nt_fwd, _mm_nt_bwd)


@jax.custom_vjp
def _saved_inverse(m, t_inv):
    del m
    return t_inv


def _saved_inverse_fwd(m, t_inv):
    del m
    return t_inv, t_inv


def _saved_inverse_bwd(t_inv, g):
    return -_dot3(_dot3(t_inv, g, 0, 0), t_inv, 1, 1), jnp.zeros_like(t_inv)


_saved_inverse.defvjp(_saved_inverse_fwd, _saved_inverse_bwd)


def _sigmoid(z):
    return 1.0 / (1.0 + jnp.exp(-z))


def _softplus(z):
    return jnp.maximum(z, 0.0) + jnp.log(1.0 + jnp.exp(-jnp.abs(z)))


def _silu(z):
    return z * _sigmoid(z)


def _rms_scale(x):
    return lax.rsqrt(jnp.mean(x * x, axis=-1, keepdims=True) + EPS)


def _rms_bwd(x, w, g):
    r = _rms_scale(x)
    gw = g * w
    dx = r * gw - x * (r * r * r) * jnp.mean(gw * x, axis=-1, keepdims=True)
    return dx, g * x * r


def _matmul(a, b, *, name, ta=False, tb=False, tm=512, tn=512, tk=512, out_dtypes=(F32,), b3=False, o3=False,
            extra=(), epilogue=None, exchange=None):
    m, k = (a.shape[1], a.shape[0]) if ta else a.shape
    if b3:
        n = b.shape[1] if tb else b.shape[0] * b.shape[2]
        kb = b.shape[0] * b.shape[2] if tb else b.shape[1]
    else:
        n, kb = (b.shape[0], b.shape[1]) if tb else (b.shape[1], b.shape[0])
    assert kb == k, (name, kb, k)
    tm, tn, tk = min(tm, m), min(tn, n), min(tk, k)
    assert m % tm == 0 and n % tn == 0 and k % tk == 0, (name, m, n, k, tm, tn, tk)
    nk = k // tk
    whole_k_blocks = b3 and tb and not ta and nk == 1 and b.shape[0] > 1
    n_extra = len(extra)
    n_out = len(out_dtypes)
    grid = (m // tm, n // tn, nk)
    ex_in, ex_in_specs, ex_out_specs, ex_out_shape, ex_scratch = _hosted(exchange)

    def body(*refs):
        a_ref, b_ref = refs[0], refs[1]
        extra_refs = refs[2:2 + n_extra]
        first_out = 2 + n_extra + len(ex_in)
        out_refs = refs[first_out:first_out + n_out]
        ex_refs = refs[2 + n_extra:first_out] + refs[first_out + n_out:first_out + n_out + len(ex_out_shape)] + refs[-2:]
        step = [pl.program_id(d) for d in range(3)]

        if exchange is not None:
            @pl.when((step[0] == 0) & (step[1] == 0) & (step[2] == 0))
            def _():
                exchange.start(*exchange.split(ex_refs))

        def finish(acc):
            outs = (acc,) if epilogue is None else epilogue(acc, *[r[...] for r in extra_refs])
            for o_ref, val in zip(out_refs, outs):
                o_ref[...] = val.astype(o_ref.dtype)

        if whole_k_blocks:
            width = b.shape[2]
            part = _dot(a_ref[:, 0:width], b_ref[0], 1, 1)
            for blk in range(1, b.shape[0]):
                part = part + _dot(a_ref[:, blk * width:(blk + 1) * width], b_ref[blk], 1, 1)
        else:
            part = _dot(a_ref[...], b_ref[...], 0 if ta else 1, 1 if tb else 0)
        if nk == 1:
            finish(part)
        else:
            acc_ref = refs[first_out + n_out + len(ex_out_shape)]

            @pl.when(step[2] == 0)
            def _():
                acc_ref[...] = part

            @pl.when(step[2] > 0)
            def _():
                acc_ref[...] += part

            @pl.when(step[2] == nk - 1)
            def _():
                finish(acc_ref[...])

        if exchange is not None:
            flat = (step[0] * grid[1] + step[1]) * nk + step[2]
            total = grid[0] * grid[1] * nk

            @pl.when(flat == total // 2)
            def _():
                exchange.middle(*exchange.split(ex_refs))

            @pl.when(flat == total - 1)
            def _():
                exchange.rest(*exchange.split(ex_refs))

    a_spec = pl.BlockSpec((tk, tm), lambda i, j, kk: (kk, i)) if ta else pl.BlockSpec((tm, tk), lambda i, j, kk: (i, kk))
    if whole_k_blocks:
        b_spec = pl.BlockSpec((b.shape[0], tn, b.shape[2]), lambda i, j, kk: (0, j, 0))
    elif b3 and tb:
        assert b.shape[2] == tk
        b_spec = pl.BlockSpec((None, tn, tk), lambda i, j, kk: (kk, j, 0))
    elif b3:
        assert b.shape[2] == tn
        b_spec = pl.BlockSpec((None, tk, tn), lambda i, j, kk: (j, kk, 0))
    elif tb:
        b_spec = pl.BlockSpec((tn, tk), lambda i, j, kk: (j, kk))
    else:
        b_spec = pl.BlockSpec((tk, tn), lambda i, j, kk: (kk, j))
    tile = pl.BlockSpec((tm, tn), lambda i, j, kk: (i, j))
    out_specs = [tile] * n_out
    out_shape = [jax.ShapeDtypeStruct((m, n), dt) for dt in out_dtypes]
    if o3:
        out_specs[0] = pl.BlockSpec((None, tm, tn), lambda i, j, kk: (j, i, 0))
        out_shape[0] = jax.ShapeDtypeStruct((n // tn, m, tn), out_dtypes[0])
    res = pl.pallas_call(
        body, name=name, grid=grid,
        in_specs=[a_spec, b_spec] + [tile] * n_extra + ex_in_specs, out_specs=out_specs + ex_out_specs,
        out_shape=out_shape + ex_out_shape,
        scratch_shapes=([pltpu.VMEM((tm, tn), F32)] if nk > 1 else []) + ex_scratch,
        compiler_params=_cparams(),
    )(a, b, *extra, *ex_in)
    if exchange is not None:
        return (res[0] if n_out == 1 else res[:n_out]), res[n_out:]
    return res[0] if n_out == 1 else res


TR = 256


def _row_spec(cols):
    return pl.BlockSpec((TR, cols), lambda i: (i, 0))


def _vec_spec(cols):
    return pl.BlockSpec((1, cols), lambda i: (0, 0))


def _pre_norm(x, w, exchange=None):
    ex_in, ex_in_specs, ex_out_specs, ex_out_shape, ex_scratch = _hosted(exchange)

    def body(*refs):
        x_ref, w_ref, h_ref = refs[0], refs[1], refs[2 + len(ex_in)]
        ex_refs = refs[2:2 + len(ex_in)] + refs[3 + len(ex_in):]
        if exchange is not None:
            @pl.when(pl.program_id(0) == 0)
            def _():
                exchange.start(*exchange.split(ex_refs))

        xv = x_ref[...]
        h_ref[...] = (xv * _rms_scale(xv) * w_ref[...]).astype(BF16)

        if exchange is not None:
            @pl.when(pl.program_id(0) == S // TR - 1)
            def _():
                exchange.finish(*exchange.split(ex_refs))

    res = pl.pallas_call(
        body, name="pre_norm", grid=(S // TR,), in_specs=[_row_spec(D), _vec_spec(D)] + ex_in_specs,
        out_specs=[_row_spec(D)] + ex_out_specs, out_shape=[jax.ShapeDtypeStruct((S, D), BF16)] + ex_out_shape,
        scratch_shapes=ex_scratch, compiler_params=_cparams(),
    )(x, w, *ex_in)
    return res[0], res[1:]


def _post_mix(x, mixed, w_post, w_pre_mlp):
    def body(x_ref, m_ref, wp_ref, wm_ref, x1_ref, h2_ref):
        mv = m_ref[...]
        x1 = x_ref[...] + mv * _rms_scale(mv) * wp_ref[...]
        x1_ref[...] = x1
        h2_ref[...] = (x1 * _rms_scale(x1) * wm_ref[...]).astype(BF16)

    return pl.pallas_call(
        body, name="post_mix", grid=(S // TR,),
        in_specs=[_row_spec(D), _row_spec(D), _vec_spec(D), _vec_spec(D)], out_specs=[_row_spec(D), _row_spec(D)],
        out_shape=[jax.ShapeDtypeStruct((S, D), F32), jax.ShapeDtypeStruct((S, D), BF16)], compiler_params=_cparams(),
    )(x, mixed, w_post, w_pre_mlp)


def _loss_head(x1, y, w_post_mlp, target):
    def body(x1_ref, y_ref, w_ref, t_ref, dx2_ref, dy_ref, dw_ref, loss_ref):
        i = pl.program_id(0)
        yv = y_ref[...]
        w = w_ref[...]
        x2 = x1_ref[...] + yv * _rms_scale(yv) * w
        err = x2 - t_ref[...]
        dx2 = err * (1.0 / D)
        dx2_ref[...] = dx2
        dy, dwt = _rms_bwd(yv, w, dx2)
        dy_ref[...] = dy.astype(BF16)

        @pl.when(i == 0)
        def _():
            dw_ref[...] = jnp.zeros_like(dw_ref)
            loss_ref[...] = jnp.zeros_like(loss_ref)

        dw_ref[...] += jnp.sum(dwt, axis=0, keepdims=True)
        part = 0.5 * jnp.sum(jnp.mean(err * err, axis=-1, keepdims=True), axis=0, keepdims=True)
        loss_ref[...] += jnp.broadcast_to(part, loss_ref.shape)

    return pl.pallas_call(
        body, name="loss_head", grid=(S // TR,),
        in_specs=[_row_spec(D), _row_spec(D), _vec_spec(D), _row_spec(D)],
        out_specs=[_row_spec(D), _row_spec(D), _vec_spec(D), _vec_spec(LANES)],
        out_shape=[jax.ShapeDtypeStruct((S, D), F32), jax.ShapeDtypeStruct((S, D), BF16),
                   jax.ShapeDtypeStruct((1, D), F32), jax.ShapeDtypeStruct((1, LANES), F32)],
        compiler_params=_cparams(),
    )(x1, y, w_post_mlp, target)


def _mid_bwd(dh2, x1, w_pre_mlp, dx2, mixed, w_post):
    def body(dh2_ref, x1_ref, wm_ref, dx2_ref, m_ref, wp_ref, dx1_ref, dm_ref, dwm_ref, dwp_ref):
        i = pl.program_id(0)
        dxa, dwm = _rms_bwd(x1_ref[...], wm_ref[...], dh2_ref[...])
        dx1 = dx2_ref[...] + dxa
        dx1_ref[...] = dx1
        dm, dwp = _rms_bwd(m_ref[...], wp_ref[...], dx1)
        dm_ref[...] = dm.astype(BF16)

        @pl.when(i == 0)
        def _():
            dwm_ref[...] = jnp.zeros_like(dwm_ref)
            dwp_ref[...] = jnp.zeros_like(dwp_ref)

        dwm_ref[...] += jnp.sum(dwm, axis=0, keepdims=True)
        dwp_ref[...] += jnp.sum(dwp, axis=0, keepdims=True)

    return pl.pallas_call(
        body, name="mid_bwd", grid=(S // TR,),
        in_specs=[_row_spec(D), _row_spec(D), _vec_spec(D), _row_spec(D), _row_spec(D), _vec_spec(D)],
        out_specs=[_row_spec(D), _row_spec(D), _vec_spec(D), _vec_spec(D)],
        out_shape=[jax.ShapeDtypeStruct((S, D), F32), jax.ShapeDtypeStruct((S, D), BF16),
                   jax.ShapeDtypeStruct((1, D), F32), jax.ShapeDtypeStruct((1, D), F32)],
        compiler_params=_cparams(),
    )(dh2, x1, w_pre_mlp, dx2, mixed, w_post)


def _pre_norm_bwd(dh, x, w, dx1):
    def body(dh_ref, x_ref, w_ref, dx1_ref, dx_ref, dw_ref):
        i = pl.program_id(0)
        dxa, dwt = _rms_bwd(x_ref[...], w_ref[...], dh_ref[...])
        dx_ref[...] = dx1_ref[...] + dxa

        @pl.when(i == 0)
        def _():
            dw_ref[...] = jnp.zeros_like(dw_ref)

        dw_ref[...] += jnp.sum(dwt, axis=0, keepdims=True)

    return pl.pallas_call(
        body, name="pre_norm_bwd", grid=(S // TR,),
        in_specs=[_row_spec(D), _row_spec(D), _vec_spec(D), _row_spec(D)], out_specs=[_row_spec(D), _vec_spec(D)],
        out_shape=[jax.ShapeDtypeStruct((S, D), F32), jax.ShapeDtypeStruct((1, D), F32)], compiler_params=_cparams(),
    )(dh, x, w, dx1)


BQ = 512
NQ = S // BQ
LANE_BETA, LANE_G = 8, 12


def _gate_lanes(shape):
    lane = lax.broadcasted_iota(jnp.int32, shape, 1)
    return lane < LANE_BETA, (lane >= LANE_BETA) & (lane < LANE_G), (lane >= LANE_G) & (lane < LANE_G + NGH)


def _gates(proj, bias_vec, alog_vec):
    def body(s_ref, b_ref, a_ref, o_ref, carry_ref):
        i = pl.program_id(0)

        @pl.when(i == 0)
        def _():
            carry_ref[...] = jnp.zeros_like(carry_ref)

        z = s_ref[...] + b_ref[...]
        tail = jnp.log(1.0 + jnp.exp(-jnp.abs(z)))
        sp = jnp.maximum(z, 0.0) + tail
        lf = jnp.minimum(z, 0.0) - tail
        r = lax.broadcasted_iota(jnp.int32, (BQ, BQ), 0)
        c = lax.broadcasted_iota(jnp.int32, (BQ, BQ), 1)
        tri = (c <= r).astype(F32)
        cum = _hdot(tri, lf) + carry_ref[...]
        carry_ref[...] = cum[BQ - 1:BQ, :]
        is_fox, is_beta, is_g = _gate_lanes(z.shape)
        o_ref[...] = jnp.where(is_fox, cum, jnp.where(is_beta, _sigmoid(z), jnp.where(is_g, -jnp.exp(a_ref[...]) * sp, 0.0)))

    return pl.pallas_call(
        body, name="gates", grid=(NQ,),
        in_specs=[pl.BlockSpec((BQ, LANES), lambda i: (i, BLK_SMALL)), _vec_spec(LANES), _vec_spec(LANES)],
        out_specs=pl.BlockSpec((BQ, LANES), lambda i: (i, 0)), out_shape=jax.ShapeDtypeStruct((S, LANES), F32),
        scratch_shapes=[pltpu.VMEM((1, LANES), F32)], compiler_params=_cparams(),
    )(proj, bias_vec, alog_vec)


def _gates_bwd(proj, bias_vec, alog_vec, dgates_gdn, dcum_fox, dproj):
    def body(s_ref, b_ref, a_ref, dg_ref, dc_ref, dproj_in, dproj_ref, red_ref, carry_ref):
        del dproj_in
        i = pl.program_id(0)

        @pl.when(i == 0)
        def _():
            carry_ref[...] = jnp.zeros_like(carry_ref)
            red_ref[...] = jnp.zeros_like(red_ref)

        z = s_ref[...] + b_ref[...]
        dg = dg_ref[...] + dc_ref[...]
        r = lax.broadcasted_iota(jnp.int32, (BQ, BQ), 0)
        c = lax.broadcasted_iota(jnp.int32, (BQ, BQ), 1)
        upper = (c >= r).astype(F32)
        dlf = _hdot(upper, dg) + carry_ref[...]
        carry_ref[...] = dlf[0:1, :]
        sig = _sigmoid(z)
        g_scale = -jnp.exp(a_ref[...])
        is_fox, is_beta, is_g = _gate_lanes(z.shape)
        ds = jnp.where(is_fox, dlf * (1.0 - sig), jnp.where(is_beta, dg * sig * (1.0 - sig), jnp.where(is_g, dg * g_scale * sig, 0.0)))
        dproj_ref[:, 0:LANES] = ds.astype(BF16)
        dproj_ref[:, LANES:2 * LANES] = jnp.zeros((BQ, LANES), BF16)
        dalog = jnp.where(is_g, dg * g_scale * _softplus(z), 0.0)
        sums = jnp.sum(ds, axis=0, keepdims=True)
        red_ref[0:1, :] += jnp.where(is_fox[0:1], sums, 0.0)
        red_ref[1:2, :] += pltpu.roll(jnp.where(is_g[0:1], sums, 0.0), LANES - LANE_G, 1)
        red_ref[2:3, :] += pltpu.roll(jnp.sum(dalog, axis=0, keepdims=True), LANES - LANE_G, 1)

    blk = pl.BlockSpec((BQ, LANES), lambda i: (NQ - 1 - i, 0))
    return pl.pallas_call(
        body, name="gates_bwd", grid=(NQ,),
        in_specs=[pl.BlockSpec((BQ, LANES), lambda i: (NQ - 1 - i, BLK_SMALL)), _vec_spec(LANES), _vec_spec(LANES), blk, blk,
                  pl.BlockSpec(memory_space=pl.ANY)],
        out_specs=[pl.BlockSpec((BQ, 2 * LANES), lambda i: (NQ - 1 - i, BLK_SMALL // 2)), pl.BlockSpec((8, LANES), lambda i: (0, 0))],
        out_shape=[jax.ShapeDtypeStruct((S, DPROJ_PAD), BF16), jax.ShapeDtypeStruct((8, LANES), F32)],
        input_output_aliases={5: 0},
        scratch_shapes=[pltpu.VMEM((1, LANES), F32)], compiler_params=_cparams(),
    )(proj, bias_vec, alog_vec, dgates_gdn, dcum_fox, dproj)


FOX_SCALE = FHD ** -0.5
FOX_PAIRS = 2
FOX_PAIRS_BWD = 2


def _head_mask(e):
    lane = lax.broadcasted_iota(jnp.int32, (1, LANES), 1)
    return (lane >= e * FHD) & (lane < (e + 1) * FHD)


def _lane_col(vals, index):
    lane = lax.broadcasted_iota(jnp.int32, vals.shape, 1)
    return jnp.sum(jnp.where(lane == index, vals, 0.0), axis=1, keepdims=True)


def _sublane_row(vals, index):
    row = lax.broadcasted_iota(jnp.int32, vals.shape, 0)
    return jnp.sum(jnp.where(row == index, vals, 0.0), axis=0, keepdims=True)


def _pair_cols(c0, c1):
    lane = lax.broadcasted_iota(jnp.int32, (c0.shape[0], 2), 1)
    return jnp.where(lane == 0, c0, c1)


def _split3(x):
    hi = x.astype(BF16).astype(F32)
    rest = x - hi
    mid = rest.astype(BF16).astype(F32)
    return hi, mid, (rest - mid).astype(BF16).astype(F32)


def _fox_operand(vals, e, cum, is_query):
    lane = lax.broadcasted_iota(jnp.int32, (1, LANES), 1)
    base = (1 - e) * FHD
    parts = _split3(cum)
    own = jnp.where(_head_mask(e), vals * FOX_SCALE if is_query else vals, 0.0)
    cum_at, ones_at = (base, base + 3) if is_query else (base + 3, base)
    sign = 1.0 if is_query else -1.0
    out = own + jnp.where((lane >= ones_at) & (lane < ones_at + 3), 1.0, 0.0)
    for i, part in enumerate(parts):
        out = out + jnp.where(lane == cum_at + i, sign * part, 0.0)
    return out.astype(BF16)


def _causal_block():
    return lax.broadcasted_iota(jnp.int32, (BQ, BQ), 1) <= lax.broadcasted_iota(jnp.int32, (BQ, BQ), 0)


def _head_rms(o, masks):
    o2 = o * o
    r = [lax.rsqrt(jnp.sum(jnp.where(mk, o2, 0.0), axis=1, keepdims=True) * (1.0 / FHD) + EPS) for mk in masks]
    return jnp.where(masks[0], r[0], r[1])


def _hosted(exchange):
    if exchange is None:
        return [], [], [], [], []
    return (exchange.inputs, [HBM] * len(exchange.inputs), [HBM] * len(exchange.out_shape), exchange.out_shape,
            exchange.sem_shapes())


def _fox_fwd(proj, gates, w2, exchange=None):
    ex_in, ex_in_specs, ex_out_specs, ex_out_shape, ex_scratch = _hosted(exchange)

    n_in = 3 * FOX_PAIRS + 2
    heads = [(pp, e) for pp in range(FOX_PAIRS) for e in range(2)]

    def body(*refs):
        qkv_refs, g_ref, w_ref = refs[:3 * FOX_PAIRS], refs[3 * FOX_PAIRS], refs[3 * FOX_PAIRS + 1]
        mix_ref, o_ref, lse_ref = refs[n_in + len(ex_in):n_in + 3 + len(ex_in)]
        ka_ref, vb_ref = refs[n_in + 3 + len(ex_in) + len(ex_out_shape):n_in + 5 + len(ex_in) + len(ex_out_shape)]
        ex_refs = refs[n_in:n_in + len(ex_in)] + refs[n_in + 3 + len(ex_in):n_in + 3 + len(ex_in) + len(ex_out_shape)] + refs[-2:]
        grp, qi = pl.program_id(0), pl.program_id(1)

        def head_index(pp, e):
            return 2 * (FOX_PAIRS * grp + pp) + e

        if exchange is not None:
            @pl.when((grp == 0) & (qi == 0))
            def _():
                exchange.start(*exchange.split(ex_refs))

        @pl.when(qi == 0)
        def _():
            gt = g_ref[...]
            for pp in range(FOX_PAIRS):
                kv = qkv_refs[3 * pp + 1][...]
                for e in range(2):
                    ka_ref[2 * pp + e] = _fox_operand(kv, e, _lane_col(gt, head_index(pp, e)), False)
                vb_ref[pp] = qkv_refs[3 * pp + 2][...].astype(BF16)

        masks = [_head_mask(0), _head_mask(1)]
        gt = g_ref[pl.ds(pl.multiple_of(qi * BQ, BQ), BQ), :]
        qs = [_fox_operand(qkv_refs[3 * pp][...], e, _lane_col(gt, head_index(pp, e)), True) for pp, e in heads]
        n = range(len(heads))

        def block(kj, carry, diagonal):
            rows = pl.ds(pl.multiple_of(kj * BQ, BQ), BQ)
            s = [_dot(qs[i], ka_ref[i, rows, :], 1, 1) for i in n]
            if diagonal:
                s = [jnp.where(_causal_block(), s[i], -jnp.inf) for i in n]
            m_new = [jnp.maximum(carry[i][0], jnp.max(s[i], axis=-1, keepdims=True)) for i in n]
            p = [jnp.exp(s[i] - m_new[i]) for i in n]
            alpha = [jnp.exp(carry[i][0] - m_new[i]) for i in n]
            l_new = [alpha[i] * carry[i][1] + jnp.sum(p[i], axis=-1, keepdims=True) for i in n]
            pv = [_dot(p[i], vb_ref[heads[i][0], rows, :]) for i in n]
            return tuple((m_new[i], l_new[i], alpha[i] * carry[i][2] + pv[i]) for i in n)

        one = (jnp.full((BQ, 1), -jnp.inf, F32), jnp.zeros((BQ, 1), F32), jnp.zeros((BQ, LANES), F32))
        below = lax.fori_loop(0, qi, lambda kj, carry: block(kj, carry, False), (one,) * len(heads))
        done = block(qi, below, True)
        for pp in range(FOX_PAIRS):
            (m0, l0, a0), (m1, l1, a1) = done[2 * pp], done[2 * pp + 1]
            o = jnp.where(masks[0], a0 / l0, a1 / l1)
            cols = slice(pp * LANES, (pp + 1) * LANES)
            o_ref[:, cols] = o
            mix_ref[:, cols] = (o * _head_rms(o, masks) * w_ref[...]).astype(BF16)
            lse_ref[pp] = _pair_cols(m0 + jnp.log(l0), m1 + jnp.log(l1))

        if exchange is not None:
            @pl.when((grp == NPAIR // FOX_PAIRS // 2) & (qi == 0))
            def _():
                exchange.middle(*exchange.split(ex_refs))

            @pl.when((grp == NPAIR // FOX_PAIRS - 1) & (qi == NQ - 1))
            def _():
                exchange.rest(*exchange.split(ex_refs))

    qkv_specs = []
    for pp in range(FOX_PAIRS):
        qkv_specs.append(pl.BlockSpec((BQ, LANES), lambda g, i, pp=pp: (i, 3 * (FOX_PAIRS * g + pp))))
        qkv_specs.append(pl.BlockSpec((S, LANES), lambda g, i, pp=pp: (0, 3 * (FOX_PAIRS * g + pp) + 1)))
        qkv_specs.append(pl.BlockSpec((S, LANES), lambda g, i, pp=pp: (0, 3 * (FOX_PAIRS * g + pp) + 2)))
    blk = pl.BlockSpec((BQ, FOX_PAIRS * LANES), lambda g, i: (i, g))
    res = pl.pallas_call(
        body, name="fox_fwd", grid=(NPAIR // FOX_PAIRS, NQ),
        in_specs=qkv_specs + [pl.BlockSpec((S, LANES), lambda g, i: (0, 0)), pl.BlockSpec((1, LANES), lambda g, i: (0, 0))]
        + ex_in_specs,
        out_specs=[blk, blk, pl.BlockSpec((FOX_PAIRS, BQ, 2), lambda g, i: (g, i, 0))] + ex_out_specs,
        out_shape=[jax.ShapeDtypeStruct((S, D), BF16), jax.ShapeDtypeStruct((S, DFOX), F32),
                   jax.ShapeDtypeStruct((NPAIR, S, 2), F32)] + ex_out_shape,
        scratch_shapes=[pltpu.VMEM((2 * FOX_PAIRS, S, LANES), BF16), pltpu.VMEM((FOX_PAIRS, S, LANES), BF16)] + ex_scratch,
        compiler_params=_cparams(),
    )(*([proj] * (3 * FOX_PAIRS)), gates, w2, *ex_in)
    return res[0], res[1], res[2], res[3:]


def _fox_norm_bwd(o, dmix, w2, exchange=None):
    ex_in, ex_in_specs, ex_out_specs, ex_out_shape, ex_scratch = _hosted(exchange)

    def body(*refs):
        o_ref, g_ref, w_ref = refs[:3]
        do_ref, dl_ref, dw_ref = refs[3 + len(ex_in):6 + len(ex_in)]
        ex_refs = refs[3:3 + len(ex_in)] + refs[6 + len(ex_in):]
        hp, qi = pl.program_id(0), pl.program_id(1)

        if exchange is not None:
            @pl.when((hp == 0) & (qi == 0))
            def _():
                exchange.start(*exchange.split(ex_refs))

        masks = [_head_mask(0), _head_mask(1)]
        ov = o_ref[...]
        g = g_ref[...]
        r = _head_rms(ov, masks)
        gw = g * w_ref[...]
        gwo = gw * ov
        mean = [jnp.sum(jnp.where(mk, gwo, 0.0), axis=1, keepdims=True) * (1.0 / FHD) for mk in masks]
        do = r * gw - ov * (r * r * r) * jnp.where(masks[0], mean[0], mean[1])
        do_ref[...] = do.astype(BF16)
        doo = do * ov
        dl_ref[...] = _pair_cols(*[jnp.sum(jnp.where(mk, doo, 0.0), axis=1, keepdims=True) for mk in masks])

        @pl.when((hp == 0) & (qi == 0))
        def _():
            dw_ref[...] = jnp.zeros_like(dw_ref)

        dw_ref[...] += jnp.sum(g * ov * r, axis=0, keepdims=True)

        @pl.when((hp == NPAIR - 1) & (qi == NQ - 1))
        def _():
            dw = dw_ref[...]
            dw_ref[...] = dw + pltpu.roll(dw, FHD, 1)
            if exchange is not None:
                exchange.finish(*exchange.split(ex_refs))

    blk = pl.BlockSpec((BQ, LANES), lambda hp, i: (i, hp))
    vec = pl.BlockSpec((1, LANES), lambda hp, i: (0, 0))
    res = pl.pallas_call(
        body, name="fox_norm_bwd", grid=(NPAIR, NQ), in_specs=[blk, blk, vec] + ex_in_specs,
        out_specs=[blk, pl.BlockSpec((None, BQ, 2), lambda hp, i: (hp, i, 0)), vec] + ex_out_specs,
        out_shape=[jax.ShapeDtypeStruct((S, DFOX), BF16), jax.ShapeDtypeStruct((NPAIR, S, 2), F32),
                   jax.ShapeDtypeStruct((1, LANES), F32)] + ex_out_shape,
        scratch_shapes=ex_scratch, compiler_params=_cparams(),
    )(o, dmix, w2, *ex_in)
    return res[0], res[1], res[2], res[3:]


def _fox_bwd(proj, do, gates, lse, delta, exchange=None):
    ex_in, ex_in_specs, ex_out_specs, ex_out_shape, ex_scratch = _hosted(exchange)

    pg = FOX_PAIRS_BWD
    n_in = 3 * pg + 4
    heads = [(pp, e) for pp in range(pg) for e in range(2)]

    def body(*refs):
        qkv_refs = refs[:3 * pg]
        do_ref, g_ref, lse_ref, dl_ref = refs[3 * pg:n_in]
        dproj_ref, dc_ref = refs[n_in + len(ex_in):n_in + 2 + len(ex_in)]
        qa_ref, dq_ref = refs[n_in + 2 + len(ex_in) + len(ex_out_shape):n_in + 4 + len(ex_in) + len(ex_out_shape)]
        ex_refs = refs[n_in:n_in + len(ex_in)] + refs[n_in + 2 + len(ex_in):n_in + 2 + len(ex_in) + len(ex_out_shape)] + refs[-2:]
        grp, kj = pl.program_id(0), pl.program_id(1)

        def head_index(pp, e):
            return 2 * (pg * grp + pp) + e

        if exchange is not None:
            @pl.when((grp == 0) & (kj == 0))
            def _():
                exchange.start(*exchange.split(ex_refs))

        @pl.when(kj == 0)
        def _():
            gt = g_ref[...]
            for pp in range(pg):
                qv = qkv_refs[3 * pp][...]
                for e in range(2):
                    qa_ref[2 * pp + e] = _fox_operand(qv, e, _lane_col(gt, head_index(pp, e)), True)
            dq_ref[...] = jnp.zeros_like(dq_ref)

        @pl.when((grp == 0) & (kj == 0))
        def _():
            dc_ref[...] = jnp.zeros_like(dc_ref)

        masks = [_head_mask(0), _head_mask(1)]
        krows = pl.ds(pl.multiple_of(kj * BQ, BQ), BQ)
        gk = g_ref[krows, :]
        kas = [_fox_operand(qkv_refs[3 * pp + 1][...], e, _lane_col(gk, head_index(pp, e)), False) for pp, e in heads]
        vbs = [qkv_refs[3 * pp + 2][...].astype(BF16) for pp in range(pg)]
        lane = lax.broadcasted_iota(jnp.int32, (BQ, LANES), 1)
        n = range(len(heads))

        def block(qi, carry, diagonal):
            dks, dvs, css = carry
            rows = pl.ds(pl.multiple_of(qi * BQ, BQ), BQ)
            qa = [qa_ref[i, rows, :] for i in n]
            s = [_dot(qa[i], kas[i], 1, 1) for i in n]
            if diagonal:
                s = [jnp.where(_causal_block(), s[i], -jnp.inf) for i in n]
            dov = [do_ref[rows, pp * LANES:(pp + 1) * LANES] for pp in range(pg)]
            doe = [jnp.where(masks[e], dov[pp], jnp.zeros_like(dov[pp])) for pp, e in heads]
            lse2 = [lse_ref[pp, rows, :] for pp in range(pg)]
            dl2 = [dl_ref[pp, rows, :] for pp in range(pg)]
            p = [jnp.exp(s[i] - _lane_col(lse2[heads[i][0]], heads[i][1])) for i in n]
            dp = [_dot(doe[i], vbs[heads[i][0]], 1, 1) for i in n]
            ds = [p[i] * (dp[i] - _lane_col(dl2[heads[i][0]], heads[i][1])) for i in n]
            dv_part = [_dot(p[i], doe[i], 0, 0) for i in n]
            dk_part = [_dot(ds[i], jnp.where(masks[heads[i][1]], qa[i], jnp.zeros_like(qa[i])), 0, 0) for i in n]
            dq_part = [jnp.where(masks[heads[i][1]], _dot(ds[i], kas[i]), 0.0) for i in n]
            css = tuple(css[i] + jnp.sum(ds[i], axis=0, keepdims=True) for i in n)
            dc = jnp.zeros((BQ, LANES), F32)
            for i in n:
                dc = dc + jnp.where(lane == head_index(*heads[i]), jnp.sum(ds[i], axis=1, keepdims=True), 0.0)
            for pp in range(pg):
                dq_ref[pp, rows, :] += (dq_part[2 * pp] + dq_part[2 * pp + 1]) * FOX_SCALE
            dc_ref[rows, :] += dc
            dks = tuple(dks[pp] + dk_part[2 * pp] + dk_part[2 * pp + 1] for pp in range(pg))
            dvs = tuple(dvs[pp] + dv_part[2 * pp] + dv_part[2 * pp + 1] for pp in range(pg))
            return dks, dvs, css

        zero = jnp.zeros((BQ, LANES), F32)
        first = block(kj, ((zero,) * pg, (zero,) * pg, (jnp.zeros((1, BQ), F32),) * len(heads)), True)
        dks, dvs, css = lax.fori_loop(kj + 1, NQ, lambda qi, carry: block(qi, carry, False), first)
        r = lax.broadcasted_iota(jnp.int32, (BQ, BQ), 0)
        c = lax.broadcasted_iota(jnp.int32, (BQ, BQ), 1)
        dcol = jnp.zeros((BQ, LANES), F32)
        for i in n:
            col = jnp.sum(jnp.where(r == c, css[i], 0.0), axis=1, keepdims=True)
            dcol = dcol + jnp.where(lane == head_index(*heads[i]), col, 0.0)
        dc_ref[krows, :] -= dcol
        for pp in range(pg):
            base = 3 * pp * LANES
            dproj_ref[krows, base + LANES:base + 2 * LANES] = dks[pp].astype(BF16)
            dproj_ref[krows, base + 2 * LANES:base + 3 * LANES] = dvs[pp].astype(BF16)

        @pl.when(kj == NQ - 1)
        def _():
            for pp in range(pg):
                dproj_ref[:, 3 * pp * LANES:(3 * pp + 1) * LANES] = dq_ref[pp].astype(BF16)

        if exchange is not None:
            @pl.when((grp == NPAIR // pg // 2) & (kj == 0))
            def _():
                exchange.middle(*exchange.split(ex_refs))

            @pl.when((grp == NPAIR // pg - 1) & (kj == NQ - 1))
            def _():
                exchange.rest(*exchange.split(ex_refs))

    qkv_specs = []
    for pp in range(pg):
        qkv_specs.append(pl.BlockSpec((S, LANES), lambda g, j, pp=pp: (0, 3 * (pg * g + pp))))
        qkv_specs.append(pl.BlockSpec((BQ, LANES), lambda g, j, pp=pp: (j, 3 * (pg * g + pp) + 1)))
        qkv_specs.append(pl.BlockSpec((BQ, LANES), lambda g, j, pp=pp: (j, 3 * (pg * g + pp) + 2)))
    pair = pl.BlockSpec((pg, S, 2), lambda g, j: (g, 0, 0))
    res = pl.pallas_call(
        body, name="fox_bwd", grid=(NPAIR // pg, NQ),
        in_specs=qkv_specs + [pl.BlockSpec((S, pg * LANES), lambda g, j: (0, g)), pl.BlockSpec((S, LANES), lambda g, j: (0, 0)),
                              pair, pair] + ex_in_specs,
        out_specs=[pl.BlockSpec((S, 3 * pg * LANES), lambda g, j: (0, g)), pl.BlockSpec((S, LANES), lambda g, j: (0, 0))]
        + ex_out_specs,
        out_shape=[jax.ShapeDtypeStruct((S, DPROJ_PAD), BF16), jax.ShapeDtypeStruct((S, LANES), F32)] + ex_out_shape,
        scratch_shapes=[pltpu.VMEM((2 * pg, S, LANES), BF16), pltpu.VMEM((pg, S, LANES), F32)] + ex_scratch,
        compiler_params=_cparams(),
    )(*([proj] * (3 * pg)), do, gates, lse, delta, *ex_in)
    return res[0], res[1], res[2:]


NQKV = 3 * NGH
GDN_QSCALE = GHD ** -0.5


def _shift_down(x, s):
    if s == 0:
        return x
    row = lax.broadcasted_iota(jnp.int32, x.shape, 0)
    return jnp.where(row >= s, pltpu.roll(x, s, 0), 0.0)


def _shift_up(x, s):
    if s == 0:
        return x
    n = x.shape[0]
    row = lax.broadcasted_iota(jnp.int32, x.shape, 0)
    return jnp.where(row < n - s, pltpu.roll(x, n - s, 0), 0.0)


def _conv_pre(xv, wv):
    pre = xv * wv[CONV_K - 1:CONV_K, :]
    for j in range(CONV_K - 1):
        pre = pre + _shift_down(xv, CONV_K - 1 - j) * wv[j:j + 1, :]
    return pre


def _l2_factors(b):
    return b < 2 * NGH, jnp.where(b < NGH, GDN_QSCALE, 1.0)


def _gdn_pre(proj, conv_w):
    def body(x_ref, w_ref, o_ref):
        b = pl.program_id(0)
        c = _silu(_conv_pre(x_ref[...], w_ref[...]))
        normed, scale = _l2_factors(b)
        rs = lax.rsqrt(jnp.sum(c * c, axis=-1, keepdims=True) + EPS)
        o_ref[...] = c * jnp.where(normed, rs, 1.0) * scale

    return pl.pallas_call(
        body, name="gdn_pre", grid=(NQKV,),
        in_specs=[pl.BlockSpec((S, GHD), lambda b: (0, BLK_GDN + b)), pl.BlockSpec((CONV_K, GHD), lambda b: (0, b))],
        out_specs=pl.BlockSpec((S, GHD), lambda b: (0, b)),
        out_shape=jax.ShapeDtypeStruct((S, NQKV * GHD), F32), compiler_params=_cparams(),
    )(proj, conv_w)


def _gdn_pre_bwd(proj, conv_w, dqkv, dproj):
    def body(x_ref, w_ref, dy_ref, dproj_in, dx_ref, dw_ref):
        del dproj_in
        b = pl.program_id(0)
        xv = x_ref[...]
        wv = w_ref[...]
        pre = _conv_pre(xv, wv)
        sig = _sigmoid(pre)
        c = pre * sig
        normed, scale = _l2_factors(b)
        g = dy_ref[...] * scale
        rs = lax.rsqrt(jnp.sum(c * c, axis=-1, keepdims=True) + EPS)
        dc_n = rs * g - c * (rs * rs * rs) * jnp.sum(g * c, axis=-1, keepdims=True)
        dc = jnp.where(normed, dc_n, g)
        dpre = dc * sig * (1.0 + pre * (1.0 - sig))
        dx = dpre * wv[CONV_K - 1:CONV_K, :]
        for j in range(CONV_K - 1):
            dx = dx + _shift_up(dpre, CONV_K - 1 - j) * wv[j:j + 1, :]
        dx_ref[...] = dx.astype(BF16)
        for j in range(CONV_K):
            dw_ref[j:j + 1, :] = jnp.sum(dpre * _shift_down(xv, CONV_K - 1 - j), axis=0, keepdims=True)

    return pl.pallas_call(
        body, name="gdn_pre_bwd", grid=(NQKV,),
        in_specs=[pl.BlockSpec((S, GHD), lambda b: (0, BLK_GDN + b)), pl.BlockSpec((CONV_K, GHD), lambda b: (0, b)),
                  pl.BlockSpec((None, S, GHD), lambda b: (b // NGH, 0, b % NGH)), pl.BlockSpec(memory_space=pl.ANY)],
        out_specs=[pl.BlockSpec((S, GHD), lambda b: (0, BLK_GDN + b)), pl.BlockSpec((CONV_K, GHD), lambda b: (0, b))],
        out_shape=[jax.ShapeDtypeStruct((S, DPROJ_PAD), BF16), jax.ShapeDtypeStruct((CONV_K, NQKV * GHD), F32)],
        input_output_aliases={3: 0}, compiler_params=_cparams(),
    )(proj, conv_w, dqkv, dproj)


CB = 16
NCB = NCH // CB


def _chunk_prep(qs, ks, vs, gcols, bcols, t_saved=None):
    n = range(len(qs))
    r = lax.broadcasted_iota(jnp.int32, (CHUNK, CHUNK), 0)
    c = lax.broadcasted_iota(jnp.int32, (CHUNK, CHUNK), 1)
    incl = c <= r
    eye = (r == c).astype(F32)
    grow = [jnp.sum(gcols[i] * eye, axis=0, keepdims=True) for i in n]
    gc_col = [jnp.sum(jnp.where(incl, grow[i], 0.0), axis=1, keepdims=True) for i in n]
    gc_row = [jnp.sum(jnp.where(r <= c, gcols[i], 0.0), axis=0, keepdims=True) for i in n]
    decay = [jnp.exp(jnp.where(incl, gc_col[i] - gc_row[i], -jnp.inf)) for i in n]
    kb = [ks[i] * bcols[i] for i in n]
    vb = [vs[i] * bcols[i] for i in n]
    kk = [_mm_nt(kb[i], ks[i]) for i in n]
    m = [jnp.where(c < r, kk[i] * decay[i], 0.0) for i in n]
    if t_saved is None:
        t_inv = [eye - m[i] for i in n]
        p = [_dot3(m[i], m[i]) for i in n]
        for step in range(5):
            t_inv = [t_inv[i] + _dot3(t_inv[i], p[i]) for i in n]
            if step < 4:
                p = [_dot3(p[i], p[i]) for i in n]
    else:
        t_inv = [_saved_inverse(m[i], t_saved[i]) for i in n]
    egc = [jnp.exp(gc_col[i]) for i in n]
    u = [_mm_nn(t_inv[i], vb[i]) for i in n]
    w = [_mm_nn(t_inv[i], kb[i] * egc[i]) for i in n]
    qk = [_mm_nt(qs[i], ks[i]) for i in n]
    gc_last = [gc_col[i][CHUNK - 1:CHUNK, :] for i in n]
    return [(u[i], w[i], qk[i] * decay[i], qs[i] * egc[i], ks[i] * jnp.exp(gc_last[i] - gc_col[i]), jnp.exp(gc_last[i]),
             t_inv[i]) for i in n]


def _prep_specs():
    rows = CB * CHUNK
    qs = pl.BlockSpec((rows, GHD), lambda i, h: (i, h))
    ks = pl.BlockSpec((rows, GHD), lambda i, h: (i, NGH + h))
    vs = pl.BlockSpec((rows, GHD), lambda i, h: (i, 2 * NGH + h))
    gs = pl.BlockSpec((rows, LANES), lambda i, h: (i, 0))
    a_s = pl.BlockSpec((None, rows, CHUNK), lambda i, h: (h, i, 0))
    gl_s = pl.BlockSpec((None, CB, 1, LANES), lambda i, h: (h, i, 0, 0))
    return qs, ks, vs, gs, a_s, gl_s


def _gdn_prep(qkv, gates, exchange=None):
    ex_in, ex_in_specs, ex_out_specs, ex_out_shape, ex_scratch = _hosted(exchange)

    def body(*refs):
        q_ref, k_ref, v_ref, g_ref = refs[:4]
        u_ref, w_ref, qd_ref, kd_ref, a_ref, gl_ref, t_ref = refs[4 + len(ex_in):11 + len(ex_in)]
        ex_refs = refs[4:4 + len(ex_in)] + refs[11 + len(ex_in):]
        h = pl.program_id(1)

        if exchange is not None:
            @pl.when((pl.program_id(0) == 0) & (h == 0))
            def _():
                exchange.start(*exchange.split(ex_refs))

        chunks = [pl.ds(cidx * CHUNK, CHUNK) for cidx in range(CB)]
        gts = [g_ref[rows, :] for rows in chunks]
        outs = _chunk_prep([q_ref[rows, :] for rows in chunks], [k_ref[rows, :] for rows in chunks],
                           [v_ref[rows, :] for rows in chunks], [_lane_col(gt, LANE_G + h) for gt in gts],
                           [_lane_col(gt, LANE_BETA + h) for gt in gts])
        for cidx, rows in enumerate(chunks):
            u, w, a, qd, kd, gl, t_inv = outs[cidx]
            u_ref[rows, :] = u
            w_ref[rows, :] = w
            qd_ref[rows, :] = qd
            kd_ref[rows, :] = kd
            a_ref[rows, :] = a
            t_ref[rows, :] = t_inv
            gl_ref[cidx] = jnp.broadcast_to(gl, (1, LANES))

        if exchange is not None:
            @pl.when((pl.program_id(0) == NCB - 1) & (h == NGH - 1))
            def _():
                exchange.finish(*exchange.split(ex_refs))

    qs, ks, vs, gs, a_s, gl_s = _prep_specs()
    tok = jax.ShapeDtypeStruct((S, DGDN), F32)
    sq = jax.ShapeDtypeStruct((NGH, S, CHUNK), F32)
    res = pl.pallas_call(
        body, name="gdn_prep", grid=(NCB, NGH), in_specs=[qs, ks, vs, gs] + ex_in_specs,
        out_specs=[qs, qs, qs, qs, a_s, gl_s, a_s] + ex_out_specs,
        out_shape=[tok, tok, tok, tok, sq, jax.ShapeDtypeStruct((NGH, NCH, 1, LANES), F32), sq] + ex_out_shape,
        scratch_shapes=ex_scratch, compiler_params=_cparams(),
    )(qkv, qkv, qkv, gates, *ex_in)
    return res[:7], res[7:]


def _gdn_prep_bwd(qkv, gates, t_inv, du, dw, dqd, dkd, da, dgl, exchange=None):
    ex_in, ex_in_specs, ex_out_specs, ex_out_shape, ex_scratch = _hosted(exchange)

    def body(*refs):
        q_ref, k_ref, v_ref, g_ref, t_ref, du_ref, dw_ref, dqd_ref, dkd_ref, da_ref, dgl_ref = refs[:11]
        dqkv_ref, dg_ref = refs[11 + len(ex_in):13 + len(ex_in)]
        ex_refs = refs[11:11 + len(ex_in)] + refs[13 + len(ex_in):]
        h = pl.program_id(1)

        if exchange is not None:
            @pl.when((pl.program_id(0) == 0) & (h == 0))
            def _():
                exchange.start(*exchange.split(ex_refs))

        @pl.when(h == 0)
        def _():
            dg_ref[...] = jnp.zeros_like(dg_ref)

        lane = lax.broadcasted_iota(jnp.int32, (CHUNK, LANES), 1)
        chunks = [pl.ds(cidx * CHUNK, CHUNK) for cidx in range(CB)]
        gts = [g_ref[rows, :] for rows in chunks]
        t_saved = [t_ref[rows, :] for rows in chunks]
        _, vjp = jax.vjp(lambda *args: [o[:6] for o in _chunk_prep(*args, t_saved=t_saved)],
                         [q_ref[rows, :] for rows in chunks], [k_ref[rows, :] for rows in chunks],
                         [v_ref[rows, :] for rows in chunks], [_lane_col(gt, LANE_G + h) for gt in gts],
                         [_lane_col(gt, LANE_BETA + h) for gt in gts])
        dqs, dks, dvs, dgcs, dbcs = vjp([(du_ref[rows, :], dw_ref[rows, :], da_ref[rows, :], dqd_ref[rows, :],
                                          dkd_ref[rows, :], dgl_ref[cidx][:, 0:1]) for cidx, rows in enumerate(chunks)])
        for cidx, rows in enumerate(chunks):
            dq, dk, dv, dgc, dbc = dqs[cidx], dks[cidx], dvs[cidx], dgcs[cidx], dbcs[cidx]
            dqkv_ref[0, rows, :] = dq
            dqkv_ref[1, rows, :] = dk
            dqkv_ref[2, rows, :] = dv
            dg_ref[rows, :] += jnp.where(lane == LANE_G + h, dgc, 0.0) + jnp.where(lane == LANE_BETA + h, dbc, 0.0)

        if exchange is not None:
            @pl.when((pl.program_id(0) == NCB - 1) & (h == NGH - 1))
            def _():
                exchange.finish(*exchange.split(ex_refs))

    qs, ks, vs, gs, a_s, gl_s = _prep_specs()
    res = pl.pallas_call(
        body, name="gdn_prep_bwd", grid=(NCB, NGH), in_specs=[qs, ks, vs, gs, a_s, qs, qs, qs, qs, a_s, gl_s] + ex_in_specs,
        out_specs=[pl.BlockSpec((3, CB * CHUNK, GHD), lambda i, h: (0, i, h)), gs] + ex_out_specs,
        out_shape=[jax.ShapeDtypeStruct((3, S, DGDN), F32), jax.ShapeDtypeStruct((S, LANES), F32)] + ex_out_shape,
        scratch_shapes=ex_scratch, compiler_params=_cparams(),
    )(qkv, qkv, qkv, gates, t_inv, du, dw, dqd, dkd, da, dgl, *ex_in)
    return res[0], res[1], res[2:]


def _scan_specs(nh, parts, reverse):
    wide, rows, chunks = nh * GHD, S // parts, NCH // parts

    def part(p):
        return parts - 1 - p if reverse else p

    hs = pl.BlockSpec((rows, wide), lambda g, p: (part(p), g))
    a_s = pl.BlockSpec((nh, rows, CHUNK), lambda g, p: (g, part(p), 0))
    gl_s = pl.BlockSpec((nh, chunks, 1, LANES), lambda g, p: (g, part(p), 0, 0))
    st_s = pl.BlockSpec((nh, chunks, GHD, GHD), lambda g, p: (g, part(p), 0, 0))
    gz_s = pl.BlockSpec((rows, wide), lambda g, p: (part(p), BLK_GZ // nh + g))
    mix_s = pl.BlockSpec((rows, wide), lambda g, p: (part(p), NPAIR // nh + g))
    return hs, a_s, gl_s, st_s, gz_s, mix_s


def _head_cols(hh):
    return slice(hh * GHD, (hh + 1) * GHD)


SCAN_HEADS, SCAN_PARTS = 4, 2
SCAN_HEADS_BWD, SCAN_PARTS_BWD = 2, 2


def _gdn_scan(u, w, qd, kd, a, gl, proj, w_norm, mix):
    heads = range(SCAN_HEADS)

    def body(u_ref, w_ref, qd_ref, kd_ref, a_ref, gl_ref, z_ref, wn_ref, mix_in, mix_ref, o_ref, st_ref, carry_ref):
        del mix_in

        @pl.when(pl.program_id(1) == 0)
        def _():
            carry_ref[...] = jnp.zeros_like(carry_ref)

        def step(ci, states):
            rows = pl.ds(pl.multiple_of(ci * CHUNK, CHUNK), CHUNK)
            for hh in heads:
                st_ref[hh, ci] = states[hh]
            ws = [_dot(w_ref[rows, _head_cols(hh)], states[hh]) for hh in heads]
            qs = [_dot(qd_ref[rows, _head_cols(hh)], states[hh]) for hh in heads]
            vn = [u_ref[rows, _head_cols(hh)] - ws[hh] for hh in heads]
            av = [_dot(a_ref[hh, rows, :], vn[hh]) for hh in heads]
            kv = [_dot(kd_ref[rows, _head_cols(hh)], vn[hh], 0, 0) for hh in heads]
            for hh in heads:
                o_ref[rows, _head_cols(hh)] = qs[hh] + av[hh]
            return tuple(states[hh] * gl_ref[hh, ci] + kv[hh] for hh in heads)

        last = lax.fori_loop(0, NCH // SCAN_PARTS, step, tuple(carry_ref[hh] for hh in heads))
        for hh in heads:
            carry_ref[hh] = last[hh]
            ov = o_ref[:, _head_cols(hh)]
            mix_ref[:, _head_cols(hh)] = (ov * _rms_scale(ov) * wn_ref[...] * _silu(z_ref[:, _head_cols(hh)])).astype(BF16)

    hs, a_s, gl_s, st_s, gz_s, mix_s = _scan_specs(SCAN_HEADS, SCAN_PARTS, False)
    return pl.pallas_call(
        body, name="gdn_scan", grid=(NGH // SCAN_HEADS, SCAN_PARTS),
        in_specs=[hs, hs, hs, hs, a_s, gl_s, gz_s, pl.BlockSpec((1, GHD), lambda g, p: (0, 0)),
                  pl.BlockSpec(memory_space=pl.ANY)],
        out_specs=[mix_s, hs, st_s],
        out_shape=[jax.ShapeDtypeStruct((S, D), BF16), jax.ShapeDtypeStruct((S, DGDN), F32),
                   jax.ShapeDtypeStruct((NGH, NCH, GHD, GHD), F32)],
        input_output_aliases={8: 0}, scratch_shapes=[pltpu.VMEM((SCAN_HEADS, GHD, GHD), F32)], compiler_params=_cparams(),
    )(u, w, qd, kd, a, gl, proj, w_norm, mix)


def _gdn_scan_bwd(dmix, o, proj, w_norm, u, w, qd, kd, a, gl, states, dproj, exchange=None):
    ex_in, ex_in_specs, ex_out_specs, ex_out_shape, ex_scratch = _hosted(exchange)
    groups = NGH // SCAN_HEADS_BWD

    def body(*refs):
        dy_ref, o_ref, z_ref, wn_ref, u_ref, w_ref, qd_ref, kd_ref, a_ref, gl_ref, st_ref = refs[:11]
        dz_ref, du_ref, dw_ref, dqd_ref, dkd_ref, da_ref, dgl_ref, dwn_ref = refs[12 + len(ex_in):20 + len(ex_in)]
        do_ref, carry_ref = refs[20 + len(ex_in) + len(ex_out_shape):22 + len(ex_in) + len(ex_out_shape)]
        ex_refs = refs[12:12 + len(ex_in)] + refs[20 + len(ex_in):20 + len(ex_in) + len(ex_out_shape)] + refs[-2:]
        heads = range(SCAN_HEADS_BWD)
        chunks = NCH // SCAN_PARTS_BWD

        if exchange is not None:
            @pl.when((pl.program_id(0) == 0) & (pl.program_id(1) == 0))
            def _():
                exchange.start(*exchange.split(ex_refs))

        @pl.when((pl.program_id(0) == 0) & (pl.program_id(1) == 0))
        def _():
            dwn_ref[...] = jnp.zeros_like(dwn_ref)

        @pl.when(pl.program_id(1) == 0)
        def _():
            carry_ref[...] = jnp.zeros_like(carry_ref)

        wn = wn_ref[...]
        for hh in heads:
            c = _head_cols(hh)
            ov = o_ref[:, c]
            zv = z_ref[:, c]
            g = dy_ref[:, c]
            sig = _sigmoid(zv)
            dz_ref[:, c] = (g * (ov * _rms_scale(ov) * wn) * sig * (1.0 + zv * (1.0 - sig))).astype(BF16)
            do, dwt = _rms_bwd(ov, wn, g * zv * sig)
            do_ref[:, c] = do
            dwn_ref[...] += jnp.sum(dwt, axis=0, keepdims=True)

        def step(t, dstates):
            ci = chunks - 1 - t
            rows = pl.ds(pl.multiple_of(ci * CHUNK, CHUNK), CHUNK)
            cols = [_head_cols(hh) for hh in heads]
            state = [st_ref[hh, ci] for hh in heads]
            dov = [do_ref[rows, cols[hh]] for hh in heads]
            wv = [w_ref[rows, cols[hh]] for hh in heads]
            ws = [_dot(wv[hh], state[hh]) for hh in heads]
            adov = [_dot(a_ref[hh, rows, :], dov[hh], 0, 0) for hh in heads]
            kds = [_dot(kd_ref[rows, cols[hh]], dstates[hh]) for hh in heads]
            dqd = [_dot(dov[hh], state[hh], 1, 1) for hh in heads]
            qdo = [_dot(qd_ref[rows, cols[hh]], dov[hh], 0, 0) for hh in heads]
            vn = [u_ref[rows, cols[hh]] - ws[hh] for hh in heads]
            dvn = [adov[hh] + kds[hh] for hh in heads]
            da = [_dot(dov[hh], vn[hh], 1, 1) for hh in heads]
            dkd = [_dot(vn[hh], dstates[hh], 1, 1) for hh in heads]
            dwv = [_dot(dvn[hh], state[hh], 1, 1) for hh in heads]
            wdv = [_dot(wv[hh], dvn[hh], 0, 0) for hh in heads]
            for hh in heads:
                da_ref[hh, rows, :] = da[hh]
                dqd_ref[rows, cols[hh]] = dqd[hh]
                dkd_ref[rows, cols[hh]] = dkd[hh]
                dgl = jnp.sum(jnp.sum(dstates[hh] * state[hh], axis=1, keepdims=True), axis=0, keepdims=True)
                dgl_ref[hh, ci] = jnp.broadcast_to(dgl, (1, LANES))
                du_ref[rows, cols[hh]] = dvn[hh]
                dw_ref[rows, cols[hh]] = -dwv[hh]
            return tuple(dstates[hh] * gl_ref[hh, ci] + qdo[hh] - wdv[hh] for hh in heads)

        last = lax.fori_loop(0, chunks, step, tuple(carry_ref[hh] for hh in heads))
        for hh in heads:
            carry_ref[hh] = last[hh]

        if exchange is not None:
            @pl.when((pl.program_id(0) == groups - 1) & (pl.program_id(1) == SCAN_PARTS_BWD - 1))
            def _():
                exchange.finish(*exchange.split(ex_refs))

    hs, a_s, gl_s, st_s, gz_s, mix_s = _scan_specs(SCAN_HEADS_BWD, SCAN_PARTS_BWD, True)
    vec = pl.BlockSpec((1, GHD), lambda g, p: (0, 0))
    tok = jax.ShapeDtypeStruct((S, DGDN), F32)
    res = pl.pallas_call(
        body, name="gdn_scan_bwd", grid=(groups, SCAN_PARTS_BWD),
        in_specs=[mix_s, hs, gz_s, vec, hs, hs, hs, hs, a_s, gl_s, st_s, pl.BlockSpec(memory_space=pl.ANY)] + ex_in_specs,
        out_specs=[gz_s, hs, hs, hs, hs, a_s, gl_s, vec] + ex_out_specs,
        out_shape=[jax.ShapeDtypeStruct((S, DPROJ_PAD), BF16), tok, tok, tok, tok,
                   jax.ShapeDtypeStruct((NGH, S, CHUNK), F32), jax.ShapeDtypeStruct((NGH, NCH, 1, LANES), F32),
                   jax.ShapeDtypeStruct((1, GHD), F32)] + ex_out_shape,
        input_output_aliases={11: 0},
        scratch_shapes=[pltpu.VMEM((S // SCAN_PARTS_BWD, SCAN_HEADS_BWD * GHD), F32),
                        pltpu.VMEM((SCAN_HEADS_BWD, GHD, GHD), F32)] + ex_scratch,
        compiler_params=_cparams(),
    )(dmix, o, proj, w_norm, u, w, qd, kd, a, gl, states, dproj, *ex_in)
    return res[:8], res[8:]


def _place():
    return lax.axis_index("x"), lax.axis_index("y"), lax.axis_index("c")


def _other_chips(x, y):
    return [(1 - x, y), (x, 1 - y), (1 - x, 1 - y)]


HBM = pl.BlockSpec(memory_space=pltpu.HBM)
VMEM = pl.BlockSpec(memory_space=pltpu.VMEM)


def _half_rows(ref_or_rows, half):
    rows = ref_or_rows // 2
    return pl.ds(pl.multiple_of(half * rows, rows), rows)


class _Exchange:
    def __init__(self, inputs, out_shape, n_sems, start, finish=None, middle=None, rest=None):
        self.inputs, self.out_shape, self.n_sems, self.start = inputs, out_shape, n_sems, start
        if finish is None:
            def finish(*refs):
                middle(*refs)
                rest(*refs)
        self.finish = finish
        self.middle = middle if middle is not None else (lambda *refs: None)
        self.rest = rest if rest is not None else finish

    def sem_shapes(self):
        return [pltpu.SemaphoreType.DMA((self.n_sems,)), pltpu.SemaphoreType.DMA((self.n_sems,))]

    def split(self, refs):
        n_in, n_out = len(self.inputs), len(self.out_shape)
        return refs[:n_in], refs[n_in:n_in + n_out], refs[n_in + n_out], refs[n_in + n_out + 1]


def _run_exchange(ex, name):
    def body(*refs):
        parts = ex.split(refs)
        ex.start(*parts)
        ex.finish(*parts)

    return pl.pallas_call(
        body, name=name, in_specs=[HBM] * len(ex.inputs), out_specs=[HBM] * len(ex.out_shape), out_shape=ex.out_shape,
        scratch_shapes=ex.sem_shapes(), compiler_params=_cparams(),
    )(*ex.inputs)


def _allgather_exchange(shards, whole=()):
    n, nw = len(shards), len(whole)
    slots = 8

    def plan(src, outs, send_sems, recv_sems):
        x, y, c = _place()
        via_x, via_y, diagonal = _other_chips(x, y)
        id_x, id_y, id_diagonal = [2 * chip[0] + chip[1] for chip in (via_x, via_y, diagonal)]
        me, sibling = (x, y, c), (x, y, 1 - c)

        def rows_of(a, half, quarter):
            total = src[a].shape[0]
            if quarter is None:
                return _half_rows(total, half)
            return pl.ds(pl.multiple_of(half * (total // 2) + quarter * (total // 4), total // 4), total // 4)

        def copy(a, k, chip_index, half, quarter, to, from_src=False):
            rows = rows_of(a, half, quarter)
            dst = outs[a].at[chip_index, rows]
            return pltpu.make_async_remote_copy(
                src_ref=src[a].at[rows] if from_src else dst, dst_ref=dst, send_sem=send_sems.at[slots * a + k],
                recv_sem=recv_sems.at[slots * a + k], device_id=to, device_id_type=MESH)

        def whole_copy(b, k, chip_index, to):
            return pltpu.make_async_remote_copy(
                src_ref=src[n + b], dst_ref=outs[n + b].at[chip_index], send_sem=send_sems.at[slots * n + 3 * b + k],
                recv_sem=recv_sems.at[slots * n + 3 * b + k], device_id=to, device_id_type=MESH)

        first, stages, last = [], [], []
        for a in range(n):
            first += [copy(a, 0, 2 * x + y, c, None, (*via_x, c), True), copy(a, 1, 2 * x + y, c, None, (*via_y, c), True)]
            stages.append([
                (copy(a, 0, id_x, c, None, me),
                 [copy(a, 2, id_x, c, 0, (*via_y, c)), copy(a, 4, id_x, c, None, sibling)]),
                (copy(a, 1, id_y, c, None, me),
                 [copy(a, 3, id_y, c, 1, (*via_x, c)), copy(a, 5, id_y, c, None, sibling)]),
                (copy(a, 2, id_diagonal, c, 0, me), [copy(a, 6, id_diagonal, c, 0, sibling)]),
                (copy(a, 3, id_diagonal, c, 1, me), [copy(a, 7, id_diagonal, c, 1, sibling)]),
            ])
            last += [copy(a, 4, id_x, 1 - c, None, me), copy(a, 5, id_y, 1 - c, None, me),
                     copy(a, 6, id_diagonal, 1 - c, 0, me), copy(a, 7, id_diagonal, 1 - c, 1, me)]
        for b in range(nw):
            for k, (chip, index) in enumerate(((via_x, id_x), (via_y, id_y), (diagonal, id_diagonal))):
                first.append(whole_copy(b, k, 2 * x + y, (*chip, c)))
                last.append(whole_copy(b, k, index, me))
        return first, stages, last

    def start(*refs):
        for cp in plan(*refs)[0]:
            cp.start()

    def pass_on(stages, which):
        for stage in which:
            for per_shard in stages:
                lands, onward = per_shard[stage]
                lands.wait_recv()
                for cp in onward:
                    cp.start()

    def middle(*refs):
        pass_on(plan(*refs)[1], (0, 1))

    def rest(*refs):
        first, stages, last = plan(*refs)
        pass_on(stages, (2, 3))
        for cp in last:
            cp.wait_recv()
        for cp in first + [cp for per_shard in stages for _, onward in per_shard for cp in onward]:
            cp.wait_send()

    out_shape = [jax.ShapeDtypeStruct((NCHIP,) + s.shape, s.dtype) for s in list(shards) + list(whole)]
    return _Exchange(list(shards) + list(whole), out_shape, slots * n + 3 * nw, start, middle=middle, rest=rest)


def _with_own(gathered, own):
    x, y, _ = _place()
    return lax.dynamic_update_index_in_dim(gathered, own, 2 * x + y, axis=0)


def _simple_exchange(inputs, out_shape, copies_of):
    def start(*refs):
        for cp in copies_of(*refs):
            cp.start()

    def finish(*refs):
        for cp in copies_of(*refs):
            cp.wait()

    return _Exchange(list(inputs), out_shape, len(out_shape) * 3, start, finish)


def _pair_exchange(grads):
    def copies_of(src, outs, send_sems, recv_sems):
        x, y, c = _place()
        return [pltpu.make_async_remote_copy(
            src_ref=src[a].at[:, _half_rows(src[a].shape[1], 1 - c)], dst_ref=outs[a], send_sem=send_sems.at[a],
            recv_sem=recv_sems.at[a], device_id=(x, y, 1 - c), device_id_type=MESH) for a in range(len(src))]

    return _simple_exchange(
        grads, [jax.ShapeDtypeStruct((g.shape[0], g.shape[1] // 2, g.shape[2]), g.dtype) for g in grads], copies_of)


def _pair_sum(grads, theirs, name):
    n = len(grads)

    def body(*refs):
        south = lax.axis_index("c") == 0
        for a in range(n):
            g = refs[a][...]
            half = g.shape[0] // 2
            mine = jnp.where(south, g[:half], g[half:])
            refs[2 * n + a][...] = (mine.astype(F32) + refs[n + a][...].astype(F32)).astype(BF16)

    def specs(arrs):
        return [pl.BlockSpec((None,) + g.shape[1:], lambda j: (j, 0, 0)) for g in arrs]

    return pl.pallas_call(
        body, name=name, grid=(NCHIP,), in_specs=specs(grads) + specs(theirs), out_specs=specs(theirs),
        out_shape=[jax.ShapeDtypeStruct(g.shape, BF16) for g in theirs], compiler_params=_cparams(),
    )(*grads, *theirs)


def _chip_exchange(parts):
    def copies_of(src, outs, send_sems, recv_sems):
        x, y, c = _place()
        return [pltpu.make_async_remote_copy(
            src_ref=src[a].at[2 * chip[0] + chip[1]], dst_ref=outs[a].at[k], send_sem=send_sems.at[3 * a + k],
            recv_sem=recv_sems.at[3 * a + k], device_id=(*chip, c), device_id_type=MESH)
            for a in range(len(src)) for k, chip in enumerate(_other_chips(x, y))]

    return _simple_exchange(parts, [jax.ShapeDtypeStruct((NCHIP - 1,) + p.shape[1:], p.dtype) for p in parts], copies_of)


def _chip_sum(parts, received, exchange=None):
    n = len(parts)
    steps = 4
    ex_in, ex_in_specs, ex_out_specs, ex_out_shape, ex_scratch = _hosted(exchange)

    def body(*refs):
        ex_refs = refs[2 * n:2 * n + len(ex_in)] + refs[3 * n + len(ex_in):]
        if exchange is not None:
            @pl.when(pl.program_id(0) == 0)
            def _():
                exchange.start(*exchange.split(ex_refs))

        chip = 2 * lax.axis_index("x") + lax.axis_index("y")
        for a in range(n):
            p, r = refs[a], refs[n + a]
            own = jnp.where(chip == 0, p[0], jnp.where(chip == 1, p[1], jnp.where(chip == 2, p[2], p[3])))
            refs[2 * n + len(ex_in) + a][...] = ((own.astype(F32) + r[0].astype(F32)) + r[1].astype(F32)) + r[2].astype(F32)

        if exchange is not None:
            @pl.when(pl.program_id(0) == steps - 1)
            def _():
                exchange.finish(*exchange.split(ex_refs))

    def specs(arrs):
        return [pl.BlockSpec((g.shape[0], g.shape[1] // steps, g.shape[2]), lambda i: (0, i, 0)) for g in arrs]

    out_specs = [pl.BlockSpec((g.shape[1] // steps, g.shape[2]), lambda i: (i, 0)) for g in parts]
    res = pl.pallas_call(
        body, name="grads_chip_sum", grid=(steps,), in_specs=specs(parts) + specs(received) + ex_in_specs,
        out_specs=out_specs + ex_out_specs, out_shape=[jax.ShapeDtypeStruct(g.shape[1:], F32) for g in parts] + ex_out_shape,
        scratch_shapes=ex_scratch, compiler_params=_cparams(),
    )(*parts, *received, *ex_in)
    return res[:n], res[n:]


def _pair_share(halves):
    def copies_of(src, outs, send_sems, recv_sems):
        x, y, c = _place()
        return [pltpu.make_async_remote_copy(
            src_ref=src[a], dst_ref=outs[a], send_sem=send_sems.at[a], recv_sem=recv_sems.at[a],
            device_id=(x, y, 1 - c), device_id_type=MESH) for a in range(len(src))]

    return _simple_exchange(halves, [jax.ShapeDtypeStruct(h.shape, F32) for h in halves], copies_of)


def _adamw_math(w, g, m, v):
    nm = ADAM_B1 * m + (1.0 - ADAM_B1) * g
    nv = ADAM_B2 * v + (1.0 - ADAM_B2) * jnp.square(g)
    m_hat = nm / (1.0 - ADAM_B1 ** ADAM_STEP)
    v_hat = nv / (1.0 - ADAM_B2 ** ADAM_STEP)
    return -ADAM_LR * (m_hat / (jnp.sqrt(v_hat) + ADAM_EPS) + ADAM_WD * w), nm, nv


def _adamw_big(ws, g_mine, g_theirs, ms, vs, exchange=None):
    n = len(ws)
    steps = 8
    ex_in, ex_in_specs, ex_out_specs, ex_out_shape, ex_scratch = _hosted(exchange)

    def body(*refs):
        ex_refs = refs[5 * n:5 * n + len(ex_in)] + refs[9 * n + len(ex_in):]
        outs = refs[5 * n + len(ex_in):9 * n + len(ex_in)]
        if exchange is not None:
            @pl.when(pl.program_id(0) == 0)
            def _():
                exchange.start(*exchange.split(ex_refs))

        own_half = (pl.program_id(0) // (steps // 2)) == lax.axis_index("c")
        for a in range(n):
            g = jnp.where(own_half, refs[n + a][...], refs[2 * n + a][...])
            d, nm, nv = _adamw_math(refs[a][...], g, refs[3 * n + a][...], refs[4 * n + a][...])
            outs[a][...] = g
            outs[n + a][...] = d
            outs[2 * n + a][...] = nm
            outs[3 * n + a][...] = nv

        if exchange is not None:
            @pl.when(pl.program_id(0) == steps - 1)
            def _():
                exchange.finish(*exchange.split(ex_refs))

    specs = [pl.BlockSpec((w.shape[0] // steps, w.shape[1]), lambda i: (i, 0)) for w in ws]
    half_specs = [pl.BlockSpec((g.shape[0] // (steps // 2), g.shape[1]), lambda i: (i % (steps // 2), 0)) for g in g_mine]
    shapes = [jax.ShapeDtypeStruct(w.shape, F32) for w in ws]
    res = pl.pallas_call(
        body, name="adamw_big", grid=(steps,), in_specs=specs + half_specs * 2 + specs * 2 + ex_in_specs,
        out_specs=specs * 4 + ex_out_specs, out_shape=shapes * 4 + ex_out_shape, scratch_shapes=ex_scratch,
        compiler_params=_cparams(),
    )(*ws, *g_mine, *g_theirs, *ms, *vs, *ex_in)
    return res[:n], res[n:2 * n], res[2 * n:3 * n], res[3 * n:4 * n], res[4 * n:]


def _adamw_in(w, g_mine, g_theirs, m, v):
    half = D // 2

    def body(w_ref, gm_ref, gt_ref, m_ref, v_ref, g_out, d_out, nm_out, nv_out, g_ref):
        south = lax.axis_index("c") == 0
        g_ref[0:half, :] = jnp.where(south, gm_ref[...], gt_ref[...])
        g_ref[half:D, :] = jnp.where(south, gt_ref[...], gm_ref[...])
        g = g_ref[0:CW, :]
        d, nm, nv = _adamw_math(w_ref[...], g, m_ref[...], v_ref[...])
        g_out[...] = g
        d_out[...] = d
        nm_out[...] = nm
        nv_out[...] = nv

    spec = pl.BlockSpec((CW, LANES), lambda i: (0, i))
    half_spec = pl.BlockSpec((half, LANES), lambda i: (0, i))
    return pl.pallas_call(
        body, name="adamw_in", grid=(D // LANES,), in_specs=[spec, half_spec, half_spec, spec, spec], out_specs=[spec] * 4,
        out_shape=[jax.ShapeDtypeStruct((CW, D), F32)] * 4, scratch_shapes=[pltpu.VMEM((D, LANES), F32)],
        compiler_params=_cparams(),
    )(w, g_mine, g_theirs, m, v)


NORM_NAMES = ("pre_mix_norm", "post_mix_norm", "pre_mlp_norm", "post_mlp_norm")
SMALL_NAMES = NORM_NAMES + ("gdn_conv_w", "fox_f_bias", "gdn_dt_bias", "gdn_a_log", "fox_out_norm", "gdn_out_norm")
CONV_COLS = 3 * DGDN // NCHIP


def _small_gather(d_norms, d_conv, sums, d_fox_norm, d_gdn_norm, loss_row):
    n_arrays = 6
    n_remote = n_arrays * (NDEV - 1)

    def copies_of(src, outs, send_sems, recv_sems):
        x, y, c = _place()
        me = 4 * x + 2 * y + c

        def from_me(chip_index):
            cols = pl.ds(pl.multiple_of(chip_index * CONV_COLS, LANES), CONV_COLS)
            return [src[0], src[1].at[:, cols], src[2], src[3], src[4], src[5]]

        local = [pltpu.make_async_copy(s, outs[a].at[me], send_sems.at[n_remote + a]) for a, s in enumerate(from_me(2 * x + y))]
        remote = []
        for k in range(1, NDEV):
            px, py, pc = x ^ ((k >> 2) & 1), y ^ ((k >> 1) & 1), c ^ (k & 1)
            remote += [pltpu.make_async_remote_copy(
                src_ref=s, dst_ref=outs[a].at[me], send_sem=send_sems.at[n_arrays * (k - 1) + a],
                recv_sem=recv_sems.at[n_arrays * (k - 1) + a], device_id=(px, py, pc), device_id_type=MESH)
                for a, s in enumerate(from_me(2 * px + py))]
        return local + remote

    def start(*refs):
        for cp in copies_of(*refs):
            cp.start()

    def finish(*refs):
        for cp in copies_of(*refs):
            cp.wait()

    shapes = [(4, D), (CONV_K, CONV_COLS), (8, LANES), (1, LANES), (1, LANES), (1, LANES)]
    return _Exchange([d_norms, d_conv, sums, d_fox_norm, d_gdn_norm, loss_row],
                     [jax.ShapeDtypeStruct((NDEV,) + s, F32) for s in shapes], n_remote + n_arrays, start, finish)


def _small_adamw(gathered, ws, ms, vs):
    n = len(SMALL_NAMES)
    ng = len(gathered)

    def body(*refs):
        def total(buf):
            acc = buf[0]
            for i in range(1, NDEV):
                acc = acc + buf[i]
            return acc

        t_norms, t_conv, t_sums, t_fn, t_gn, t_loss = [total(r) for r in refs[:ng]]
        w_refs, m_refs, v_refs = refs[ng:ng + n], refs[ng + n:ng + 2 * n], refs[ng + 2 * n:ng + 3 * n]
        outs = refs[ng + 3 * n:]
        outs[4 * n][...] = t_loss
        grads = [t_norms[i:i + 1, :] for i in range(4)] + [
            t_conv, t_sums[0:1, 0:NFH], t_sums[1:2, 0:NGH], t_sums[2:3, 0:NGH], t_fn[:, 0:FHD], t_gn]
        for a in range(n):
            d, nm, nv = _adamw_math(w_refs[a][...], grads[a], m_refs[a][...], v_refs[a][...])
            outs[a][...] = grads[a]
            outs[n + a][...] = d
            outs[2 * n + a][...] = nm
            outs[3 * n + a][...] = nv

    def whole(arr):
        return pl.BlockSpec(arr.shape, lambda i: (0,) * arr.ndim)

    res = pl.pallas_call(
        body, name="small_adamw", grid=(1,), in_specs=[whole(t) for t in gathered] + [whole(w) for w in ws] * 3,
        out_specs=[whole(w) for w in ws] * 4 + [pl.BlockSpec((1, LANES), lambda i: (0, 0))],
        out_shape=[jax.ShapeDtypeStruct(w.shape, F32) for w in ws] * 4 + [jax.ShapeDtypeStruct((1, LANES), F32)],
        compiler_params=_cparams(),
    )(*gathered, *ws, *ms, *vs)
    return res[:n], res[n:2 * n], res[2 * n:3 * n], res[3 * n:4 * n], res[4 * n]


CW = DPROJ // NCHIP
PROJ_RUNS = tuple((part * DFOX + hp * LANES, part * DFOX + (hp + 1) * LANES, (3 * hp + part) * LANES)
                  for hp in range(NPAIR) for part in range(3)) + (
    (1536, 1544, BLK_SMALL * LANES), (1544, 3080, BLK_GDN * LANES), (3080, 3088, BLK_SMALL * LANES + 8),
    (3088, 3600, BLK_GZ * LANES))


def _proj_pieces():
    pieces = []
    for lo, hi, at in PROJ_RUNS:
        while lo < hi:
            j = lo // CW
            end = min(hi, (j + 1) * CW)
            pieces.append((j, lo - j * CW, at, end - lo))
            at, lo = at + end - lo, end
    return pieces


RT = 256


def _to_padded_rows(gathered):
    def body(src_ref, out_ref, blocks_ref, rows_ref):
        blocks_ref[...] = src_ref[...].astype(F32)
        rows_ref[...] = jnp.zeros_like(rows_ref)
        for j, start, at, n in _proj_pieces():
            rows_ref[at:at + n, :] = blocks_ref[j, start:start + n, :]
        out_ref[...] = rows_ref[...].astype(out_ref.dtype)

    return pl.pallas_call(
        body, name="proj_rows_in", grid=(D // RT,), in_specs=[pl.BlockSpec((NCHIP, D, RT), lambda i: (0, 0, i))],
        out_specs=pl.BlockSpec((DPROJ_PAD, RT), lambda i: (0, i)), out_shape=jax.ShapeDtypeStruct((DPROJ_PAD, D), gathered.dtype),
        scratch_shapes=[pltpu.VMEM((NCHIP, D, RT), F32), pltpu.VMEM((DPROJ_PAD, RT), F32)], compiler_params=_cparams(),
    )(gathered)


def _from_padded_rows(w):
    def body(src_ref, out_ref, rows_ref, blocks_ref):
        rows_ref[...] = src_ref[...].astype(F32)
        blocks_ref[...] = jnp.zeros_like(blocks_ref)
        for j, start, at, n in _proj_pieces():
            blocks_ref[j, start:start + n, :] = rows_ref[at:at + n, :]
        out_ref[...] = blocks_ref[...].astype(out_ref.dtype)

    return pl.pallas_call(
        body, name="proj_rows_out", grid=(D // RT,), in_specs=[pl.BlockSpec((DPROJ_PAD, RT), lambda i: (0, i))],
        out_specs=pl.BlockSpec((NCHIP, D, RT), lambda i: (0, 0, i)), out_shape=jax.ShapeDtypeStruct((NCHIP, D, D), w.dtype),
        scratch_shapes=[pltpu.VMEM((DPROJ_PAD, RT), F32), pltpu.VMEM((NCHIP, D, RT), F32)], compiler_params=_cparams(),
    )(w)


def _local_step(x, target, first_weights, late_weights, reduce_late, reduce_in, pre_mix_norm, fox_f_bias, fox_out_norm,
                gdn_a_log, gdn_dt_bias, gdn_out_norm, post_mix_norm, pre_mlp_norm, post_mlp_norm):
    bias_vec = jnp.zeros((1, LANES), F32).at[0, 0:NFH].set(fox_f_bias).at[0, LANE_G:LANE_G + NGH].set(gdn_dt_bias)
    alog_vec = jnp.zeros((1, LANES), F32).at[0, LANE_G:LANE_G + NGH].set(gdn_a_log)
    w2 = jnp.concatenate([fox_out_norm, fox_out_norm], axis=1)

    h, first = _pre_norm(x, pre_mix_norm, exchange=first_weights[0])
    win_p, conv_w = first_weights[1](first)
    proj = _matmul(h, win_p, tb=True, tm=2048, tn=768, tk=1024, name="mm_proj", exchange=late_weights[0])
    proj, late_a = proj if late_weights[0] is not None else (proj, [])
    gates = _gates(proj, bias_vec, alog_vec)
    mix, fox_o, lse, late_b = _fox_fwd(proj, gates, w2, exchange=late_weights[1])
    qkv = _gdn_pre(proj, conv_w)
    (u, w, qd, kd, a_intra, gl, t_inv), late_c = _gdn_prep(qkv, gates, exchange=late_weights[2])
    wout, wup3, wdown = late_weights[3](late_a, late_b, late_c)
    mix, gdn_raw, states = _gdn_scan(u, w, qd, kd, a_intra, gl, proj, gdn_out_norm, mix)
    mixed = _matmul(mix, wout, tm=2048, tk=1024, name="mm_out")
    x1, h2 = _post_mix(x, mixed, post_mix_norm, pre_mlp_norm)

    def relu2(acc):
        r = jnp.maximum(acc, 0.0)
        return r, r * r

    up_relu, act = _matmul(h2, wup3, b3=True, tm=1024, tn=1024, tk=1024, out_dtypes=(BF16, BF16), epilogue=relu2,
                           name="mm_up")
    y = _matmul(act, wdown, tm=1024, tk=DFF, name="mm_down")
    dx2, dy, d_post_mlp, loss_row = _loss_head(x1, y, post_mlp_norm, target)

    dwdown = _matmul(act, dy, ta=True, tm=1024, tn=1024, tk=2048, out_dtypes=(BF16,), name="mm_dwdown")

    def relu2_bwd(acc, r):
        return (acc * 2.0 * r.astype(F32),)

    dup = _matmul(dy, wdown, tb=True, tm=1024, tn=1024, tk=1024, out_dtypes=(BF16,), extra=(up_relu,), epilogue=relu2_bwd,
                  name="mm_dact")
    dwup3 = _matmul(h2, dup, ta=True, tm=1024, tn=1024, tk=2048, out_dtypes=(BF16,), o3=True, name="mm_dwup")
    dh2 = _matmul(dup, wup3, tb=True, b3=True, tm=1024, tk=DFF, name="mm_dh2")
    dx1, dmixed, d_pre_mlp, d_post_mix = _mid_bwd(dh2, x1, pre_mlp_norm, dx2, mixed, post_mix_norm)
    dwout = _matmul(mix, dmixed, ta=True, tm=1024, tn=1024, tk=2048, out_dtypes=(BF16,), name="mm_dwout")
    dmix = _matmul(dmixed, wout, tb=True, tm=2048, tk=1024, name="mm_dmix")

    dfox, delta, d_fox_norm, from_sibling = _fox_norm_bwd(fox_o, dmix, w2, exchange=reduce_late[0](dwout, dwup3, dwdown))
    dproj, dcum_fox, reduced_a = _fox_bwd(proj, dfox, gates, lse, delta, exchange=reduce_late[1](from_sibling))
    (dproj, du, dw, dqd, dkd, da, dgl, d_gdn_norm), reduced_b = _gdn_scan_bwd(
        dmix, gdn_raw, proj, gdn_out_norm, u, w, qd, kd, a_intra, gl, states, dproj, exchange=reduce_late[2]())
    dqkv, dgates_gdn, reduced_c = _gdn_prep_bwd(qkv, gates, t_inv, du, dw, dqd, dkd, da, dgl, exchange=reduce_late[3]())
    reduced_late = (reduced_a, reduced_b, reduced_c)
    dproj, d_conv = _gdn_pre_bwd(proj, conv_w, dqkv, dproj)
    dproj, sums = _gates_bwd(proj, bias_vec, alog_vec, dgates_gdn, dcum_fox, dproj)

    dwin_p = _matmul(dproj, h, ta=True, tm=1280, tn=1024, tk=2048, out_dtypes=(BF16,), name="mm_dwin")
    exchange_in = reduce_in(dwin_p)
    dh = _matmul(dproj, win_p, tm=1024, tk=DPROJ_PAD, name="mm_dh", exchange=exchange_in)
    dh, reduced_in = dh if exchange_in is not None else (dh, [])
    grad_x, d_pre_mix = _pre_norm_bwd(dh, x, pre_mix_norm, dx1)

    d_norms = jnp.concatenate([d_pre_mix, d_post_mix, d_pre_mlp, d_post_mlp], axis=0)
    return grad_x, (d_norms, d_conv, sums, d_fox_norm, d_gdn_norm, loss_row), reduced_late, reduced_in


def kernel(x, pre_mix_norm, w_in, fox_f_bias, fox_out_norm, gdn_conv_w, gdn_a_log, gdn_dt_bias, gdn_out_norm, w_out, post_mix_norm, pre_mlp_norm, w_up, w_down, post_mlp_norm, loss_target, m_pre_mix_norm, m_w_in, m_fox_f_bias, m_fox_out_norm, m_gdn_conv_w, m_gdn_a_log, m_gdn_dt_bias, m_gdn_out_norm, m_w_out, m_post_mix_norm, m_pre_mlp_norm, m_w_up, m_w_down, m_post_mlp_norm, v_pre_mix_norm, v_w_in, v_fox_f_bias, v_fox_out_norm, v_gdn_conv_w, v_gdn_a_log, v_gdn_dt_bias, v_gdn_out_norm, v_w_out, v_post_mix_norm, v_pre_mlp_norm, v_w_up, v_w_down, v_post_mlp_norm):
    weights = dict(pre_mix_norm=pre_mix_norm, w_in=w_in, fox_f_bias=fox_f_bias, fox_out_norm=fox_out_norm, gdn_conv_w=gdn_conv_w,
                   gdn_a_log=gdn_a_log, gdn_dt_bias=gdn_dt_bias, gdn_out_norm=gdn_out_norm, w_out=w_out, post_mix_norm=post_mix_norm,
                   pre_mlp_norm=pre_mlp_norm, w_up=w_up, w_down=w_down, post_mlp_norm=post_mlp_norm)
    m_in = dict(pre_mix_norm=m_pre_mix_norm, w_in=m_w_in, fox_f_bias=m_fox_f_bias, fox_out_norm=m_fox_out_norm, gdn_conv_w=m_gdn_conv_w,
                gdn_a_log=m_gdn_a_log, gdn_dt_bias=m_gdn_dt_bias, gdn_out_norm=m_gdn_out_norm, w_out=m_w_out, post_mix_norm=m_post_mix_norm,
                pre_mlp_norm=m_pre_mlp_norm, w_up=m_w_up, w_down=m_w_down, post_mlp_norm=m_post_mlp_norm)
    v_in = dict(pre_mix_norm=v_pre_mix_norm, w_in=v_w_in, fox_f_bias=v_fox_f_bias, fox_out_norm=v_fox_out_norm, gdn_conv_w=v_gdn_conv_w,
                gdn_a_log=v_gdn_a_log, gdn_dt_bias=v_gdn_dt_bias, gdn_out_norm=v_gdn_out_norm, w_out=v_w_out, post_mix_norm=v_post_mix_norm,
                pre_mlp_norm=v_pre_mlp_norm, w_up=v_w_up, w_down=v_w_down, post_mlp_norm=v_post_mlp_norm)
    order_w = ("pre_mix_norm", "w_in", "fox_f_bias", "fox_out_norm", "gdn_conv_w", "gdn_a_log", "gdn_dt_bias", "gdn_out_norm", "w_out",
               "post_mix_norm", "pre_mlp_norm", "w_up", "w_down", "post_mlp_norm")
    big = ("w_in", "w_out", "w_up", "w_down")

    def row(v):
        return v if v.ndim == 2 else v.reshape(1, -1)

    win_shard = jnp.pad(w_in.T.astype(BF16), ((0, D - CW), (0, 0)))

    def resolve_first(gathered):
        win_g, conv_g = gathered
        return (_to_padded_rows(_with_own(win_g, win_shard)),
                _with_own(conv_g, gdn_conv_w).transpose(1, 0, 2).reshape(CONV_K, 3 * DGDN))

    late_shards = [weights[n].astype(BF16) for n in big[1:]]

    def resolve_late(gathered_out, gathered_mlp, _):
        wout_g, wup3, wdown_g = [_with_own(g, own) for g, own in zip(list(gathered_out) + list(gathered_mlp), late_shards)]
        return wout_g.reshape(D, D), wup3, wdown_g.reshape(DFF, D)

    pair_sums, late_blocks = {}, []

    def pair_summed(names, blocks, theirs):
        for n, s in zip(names, _pair_sum(blocks, theirs, "grads_pair_sum_" + names[0])):
            pair_sums[n] = s

    def late_pair_exchange(dwout, dwup3, dwdown):
        late_blocks.extend([dwout.reshape(NCHIP, D // NCHIP, D), dwup3, dwdown.reshape(NCHIP, DFF // NCHIP, D)])
        return _pair_exchange(late_blocks)

    def late_chip_exchange(theirs):
        pair_summed(big[1:], late_blocks, theirs)
        return _chip_exchange([pair_sums["w_up"], pair_sums["w_down"]])

    def reduce_in(dwin_p):
        blocks = [_from_padded_rows(dwin_p)]
        pair_summed(big[:1], blocks, _run_exchange(_pair_exchange(blocks), "grads_pair_exchange_w_in"))
        return _chip_exchange([pair_sums["w_in"]])

    grad_x, small, received_late, received_in = _local_step(
        x[0], loss_target[0], (_allgather_exchange([win_shard], whole=[gdn_conv_w]), resolve_first),
        (_allgather_exchange(late_shards[:1]), _allgather_exchange(late_shards[1:]), None, resolve_late),
        (late_pair_exchange, late_chip_exchange, lambda: None, lambda: _chip_exchange([pair_sums["w_out"]])),
        reduce_in, row(pre_mix_norm), fox_f_bias, row(fox_out_norm), gdn_a_log, gdn_dt_bias,
        row(gdn_out_norm), row(post_mix_norm), row(pre_mlp_norm), row(post_mlp_norm))
    received_mlp, _, received_out = received_late

    g_mine, small_gathered = _chip_sum(
        [pair_sums[n] for n in big], list(received_in[:1]) + list(received_out[:1]) + list(received_mlp[:2]),
        exchange=_small_gather(*small))
    g_theirs = _run_exchange(_pair_share(g_mine), "grads_pair_share")

    g_big, d_big, nm_big, nv_big, _ = _adamw_big(
        [weights[n] for n in big[1:]], g_mine[1:], g_theirs[1:], [m_in[n] for n in big[1:]], [v_in[n] for n in big[1:]])
    in_t = _adamw_in(w_in.T, g_mine[0], g_theirs[0], m_w_in.T, v_w_in.T)
    g_small, d_small, nm_small, nv_small, loss_total = _small_adamw(
        small_gathered, [row(weights[n]) for n in SMALL_NAMES], [row(m_in[n]) for n in SMALL_NAMES],
        [row(v_in[n]) for n in SMALL_NAMES])

    grads, delta, new_m, new_v = {}, {}, {}, {}
    grads["w_in"], delta["w_in"], new_m["w_in"], new_v["w_in"] = [t.T for t in in_t]
    for i, n in enumerate(big[1:]):
        grads[n], delta[n], new_m[n], new_v[n] = g_big[i], d_big[i], nm_big[i], nv_big[i]
    for i, n in enumerate(SMALL_NAMES):
        shape = weights[n].shape
        grads[n], delta[n], new_m[n], new_v[n] = (g_small[i].reshape(shape), d_small[i].reshape(shape),
                                                  nm_small[i].reshape(shape), nv_small[i].reshape(shape))
    return (loss_total[0, 0], grad_x[None], *[grads[n] for n in order_w], *[delta[n] for n in order_w], *[new_m[n] for n in order_w],
            *[new_v[n] for n in order_w])
```

```python
import jax
import jax.numpy as jnp
from jax import lax
from jax.experimental import pallas as pl
from jax.experimental.pallas import tpu as pltpu

F32 = jnp.float32
BF16 = jnp.bfloat16
MESH = pl.DeviceIdType.MESH

S = 2048
D = 1024
NFH, FHD = 8, 64
NPAIR = NFH // 2
NGH, GHD = 4, 128
DFOX = NFH * FHD
DGDN = NGH * GHD
CHUNK = 64
NCH = S // CHUNK
CONV_K = 4
DFF = 4 * D
EPS = 1e-6
DPROJ = 3600
LANES = 128
DPROJ_PAD = 3840
BLK_GDN = 12
BLK_GZ = 24
BLK_SMALL = 28
NCHIP = 4
NDEV = 8
VMEM_LIMIT = 56 * 1024 * 1024

ADAM_LR = 0.001
ADAM_B1 = 0.9
ADAM_B2 = 0.999
ADAM_EPS = 1e-08
ADAM_WD = 0.01
ADAM_STEP = 10


def _cparams(**kw):
    return pltpu.CompilerParams(vmem_limit_bytes=VMEM_LIMIT, **kw)


def _dn(ca, cb):
    return (((ca,), (cb,)), ((), ()))


def _dot(a, b, ca=1, cb=0):
    return lax.dot_general(a.astype(BF16), b.astype(BF16), _dn(ca, cb), preferred_element_type=F32)


def _hdot(a, b, ca=1, cb=0):
    return lax.dot_general(a.astype(F32), b.astype(F32), _dn(ca, cb), precision=lax.Precision.HIGHEST,
                           preferred_element_type=F32)


def _dot3(a, b, ca=1, cb=0):
    a_hi, b_hi = a.astype(BF16), b.astype(BF16)
    a_lo, b_lo = (a - a_hi.astype(F32)).astype(BF16), (b - b_hi.astype(F32)).astype(BF16)
    dn = _dn(ca, cb)
    return (lax.dot_general(a_hi, b_hi, dn, preferred_element_type=F32)
            + (lax.dot_general(a_hi, b_lo, dn, preferred_element_type=F32)
               + lax.dot_general(a_lo, b_hi, dn, preferred_element_type=F32)))


@jax.custom_vjp
def _mm_nn(a, b):
    return _dot(a, b, 1, 0)


def _mm_nn_fwd(a, b):
    return _dot(a, b, 1, 0), (a, b)


def _mm_nn_bwd(res, g):
    a, b = res
    return _dot(g, b, 1, 1), _dot(a, g, 0, 0)


_mm_nn.defvjp(_mm_nn_fwd, _mm_nn_bwd)


@jax.custom_vjp
def _mm_nt(a, b):
    return _dot(a, b, 1, 1)


def _mm_nt_fwd(a, b):
    return _dot(a, b, 1, 1), (a, b)


def _mm_nt_bwd(res, g):
    a, b = res
    return _dot(g, b, 1, 0), _dot(g, a, 0, 0)


_mm_nt.defvjp(_mm_nt_fwd, _mm_nt_bwd)


@jax.custom_vjp
def _saved_inverse(m, t_inv):
    del m
    return t_inv


def _saved_inverse_fwd(m, t_inv):
    del m
    return t_inv, t_inv


def _saved_inverse_bwd(t_inv, g):
    return -_dot3(_dot3(t_inv, g, 0, 0), t_inv, 1, 1), jnp.zeros_like(t_inv)


_saved_inverse.defvjp(_saved_inverse_fwd, _saved_inverse_bwd)


def _sigmoid(z):
    return 1.0 / (1.0 + jnp.exp(-z))


def _softplus(z):
    return jnp.maximum(z, 0.0) + jnp.log(1.0 + jnp.exp(-jnp.abs(z)))


def _silu(z):
    return z * _sigmoid(z)


def _rms_scale(x):
    return lax.rsqrt(jnp.mean(x * x, axis=-1, keepdims=True) + EPS)


def _rms_bwd(x, w, g):
    r = _rms_scale(x)
    gw = g * w
    dx = r * gw - x * (r * r * r) * jnp.mean(gw * x, axis=-1, keepdims=True)
    return dx, g * x * r


def _matmul(a, b, *, name, ta=False, tb=False, tm=512, tn=512, tk=512, out_dtypes=(F32,), b3=False, o3=False,
            extra=(), epilogue=None, exchange=None):
    m, k = (a.shape[1], a.shape[0]) if ta else a.shape
    if b3:
        n = b.shape[1] if tb else b.shape[0] * b.shape[2]
        kb = b.shape[0] * b.shape[2] if tb else b.shape[1]
    else:
        n, kb = (b.shape[0], b.shape[1]) if tb else (b.shape[1], b.shape[0])
    assert kb == k, (name, kb, k)
    tm, tn, tk = min(tm, m), min(tn, n), min(tk, k)
    assert m % tm == 0 and n % tn == 0 and k % tk == 0, (name, m, n, k, tm, tn, tk)
    nk = k // tk
    whole_k_blocks = b3 and tb and not ta and nk == 1 and b.shape[0] > 1
    n_extra = len(extra)
    n_out = len(out_dtypes)
    grid = (m // tm, n // tn, nk)
    ex_in, ex_in_specs, ex_out_specs, ex_out_shape, ex_scratch = _hosted(exchange)

    def body(*refs):
        a_ref, b_ref = refs[0], refs[1]
        extra_refs = refs[2:2 + n_extra]
        first_out = 2 + n_extra + len(ex_in)
        out_refs = refs[first_out:first_out + n_out]
        ex_refs = refs[2 + n_extra:first_out] + refs[first_out + n_out:first_out + n_out + len(ex_out_shape)] + refs[-2:]
        step = [pl.program_id(d) for d in range(3)]

        if exchange is not None:
            @pl.when((step[0] == 0) & (step[1] == 0) & (step[2] == 0))
            def _():
                exchange.start(*exchange.split(ex_refs))

        def finish(acc):
            outs = (acc,) if epilogue is None else epilogue(acc, *[r[...] for r in extra_refs])
            for o_ref, val in zip(out_refs, outs):
                o_ref[...] = val.astype(o_ref.dtype)

        if whole_k_blocks:
            width = b.shape[2]
            part = _dot(a_ref[:, 0:width], b_ref[0], 1, 1)
            for blk in range(1, b.shape[0]):
                part = part + _dot(a_ref[:, blk * width:(blk + 1) * width], b_ref[blk], 1, 1)
        else:
            part = _dot(a_ref[...], b_ref[...], 0 if ta else 1, 1 if tb else 0)
        if nk == 1:
            finish(part)
        else:
            acc_ref = refs[first_out + n_out + len(ex_out_shape)]

            @pl.when(step[2] == 0)
            def _():
                acc_ref[...] = part

            @pl.when(step[2] > 0)
            def _():
                acc_ref[...] += part

            @pl.when(step[2] == nk - 1)
            def _():
                finish(acc_ref[...])

        if exchange is not None:
            flat = (step[0] * grid[1] + step[1]) * nk + step[2]
            total = grid[0] * grid[1] * nk

            @pl.when(flat == total // 2)
            def _():
                exchange.middle(*exchange.split(ex_refs))

            @pl.when(flat == total - 1)
            def _():
                exchange.rest(*exchange.split(ex_refs))

    a_spec = pl.BlockSpec((tk, tm), lambda i, j, kk: (kk, i)) if ta else pl.BlockSpec((tm, tk), lambda i, j, kk: (i, kk))
    if whole_k_blocks:
        b_spec = pl.BlockSpec((b.shape[0], tn, b.shape[2]), lambda i, j, kk: (0, j, 0))
    elif b3 and tb:
        assert b.shape[2] == tk
        b_spec = pl.BlockSpec((None, tn, tk), lambda i, j, kk: (kk, j, 0))
    elif b3:
        assert b.shape[2] == tn
        b_spec = pl.BlockSpec((None, tk, tn), lambda i, j, kk: (j, kk, 0))
    elif tb:
        b_spec = pl.BlockSpec((tn, tk), lambda i, j, kk: (j, kk))
    else:
        b_spec = pl.BlockSpec((tk, tn), lambda i, j, kk: (kk, j))
    tile = pl.BlockSpec((tm, tn), lambda i, j, kk: (i, j))
    out_specs = [tile] * n_out
    out_shape = [jax.ShapeDtypeStruct((m, n), dt) for dt in out_dtypes]
    if o3:
        out_specs[0] = pl.BlockSpec((None, tm, tn), lambda i, j, kk: (j, i, 0))
        out_shape[0] = jax.ShapeDtypeStruct((n // tn, m, tn), out_dtypes[0])
    res = pl.pallas_call(
        body, name=name, grid=grid,
        in_specs=[a_spec, b_spec] + [tile] * n_extra + ex_in_specs, out_specs=out_specs + ex_out_specs,
        out_shape=out_shape + ex_out_shape,
        scratch_shapes=([pltpu.VMEM((tm, tn), F32)] if nk > 1 else []) + ex_scratch,
        compiler_params=_cparams(),
    )(a, b, *extra, *ex_in)
    if exchange is not None:
        return (res[0] if n_out == 1 else res[:n_out]), res[n_out:]
    return res[0] if n_out == 1 else res


TR = 256


def _row_spec(cols):
    return pl.BlockSpec((TR, cols), lambda i: (i, 0))


def _vec_spec(cols):
    return pl.BlockSpec((1, cols), lambda i: (0, 0))


def _pre_norm(x, w, exchange=None):
    ex_in, ex_in_specs, ex_out_specs, ex_out_shape, ex_scratch = _hosted(exchange)

    def body(*refs):
        x_ref, w_ref, h_ref = refs[0], refs[1], refs[2 + len(ex_in)]
        ex_refs = refs[2:2 + len(ex_in)] + refs[3 + len(ex_in):]
        if exchange is not None:
            @pl.when(pl.program_id(0) == 0)
            def _():
                exchange.start(*exchange.split(ex_refs))

        xv = x_ref[...]
        h_ref[...] = (xv * _rms_scale(xv) * w_ref[...]).astype(BF16)

        if exchange is not None:
            @pl.when(pl.program_id(0) == S // TR - 1)
            def _():
                exchange.finish(*exchange.split(ex_refs))

    res = pl.pallas_call(
        body, name="pre_norm", grid=(S // TR,), in_specs=[_row_spec(D), _vec_spec(D)] + ex_in_specs,
        out_specs=[_row_spec(D)] + ex_out_specs, out_shape=[jax.ShapeDtypeStruct((S, D), BF16)] + ex_out_shape,
        scratch_shapes=ex_scratch, compiler_params=_cparams(),
    )(x, w, *ex_in)
    return res[0], res[1:]


def _post_mix(x, mixed, w_post, w_pre_mlp):
    def body(x_ref, m_ref, wp_ref, wm_ref, x1_ref, h2_ref):
        mv = m_ref[...]
        x1 = x_ref[...] + mv * _rms_scale(mv) * wp_ref[...]
        x1_ref[...] = x1
        h2_ref[...] = (x1 * _rms_scale(x1) * wm_ref[...]).astype(BF16)

    return pl.pallas_call(
        body, name="post_mix", grid=(S // TR,),
        in_specs=[_row_spec(D), _row_spec(D), _vec_spec(D), _vec_spec(D)], out_specs=[_row_spec(D), _row_spec(D)],
        out_shape=[jax.ShapeDtypeStruct((S, D), F32), jax.ShapeDtypeStruct((S, D), BF16)], compiler_params=_cparams(),
    )(x, mixed, w_post, w_pre_mlp)


def _loss_head(x1, y, w_post_mlp, target):
    def body(x1_ref, y_ref, w_ref, t_ref, dx2_ref, dy_ref, dw_ref, loss_ref):
        i = pl.program_id(0)
        yv = y_ref[...]
        w = w_ref[...]
        x2 = x1_ref[...] + yv * _rms_scale(yv) * w
        err = x2 - t_ref[...]
        dx2 = err * (1.0 / D)
        dx2_ref[...] = dx2
        dy, dwt = _rms_bwd(yv, w, dx2)
        dy_ref[...] = dy.astype(BF16)

        @pl.when(i == 0)
        def _():
            dw_ref[...] = jnp.zeros_like(dw_ref)
            loss_ref[...] = jnp.zeros_like(loss_ref)

        dw_ref[...] += jnp.sum(dwt, axis=0, keepdims=True)
        part = 0.5 * jnp.sum(jnp.mean(err * err, axis=-1, keepdims=True), axis=0, keepdims=True)
        loss_ref[...] += jnp.broadcast_to(part, loss_ref.shape)

    return pl.pallas_call(
        body, name="loss_head", grid=(S // TR,),
        in_specs=[_row_spec(D), _row_spec(D), _vec_spec(D), _row_spec(D)],
        out_specs=[_row_spec(D), _row_spec(D), _vec_spec(D), _vec_spec(LANES)],
        out_shape=[jax.ShapeDtypeStruct((S, D), F32), jax.ShapeDtypeStruct((S, D), BF16),
                   jax.ShapeDtypeStruct((1, D), F32), jax.ShapeDtypeStruct((1, LANES), F32)],
        compiler_params=_cparams(),
    )(x1, y, w_post_mlp, target)


def _mid_bwd(dh2, x1, w_pre_mlp, dx2, mixed, w_post):
    def body(dh2_ref, x1_ref, wm_ref, dx2_ref, m_ref, wp_ref, dx1_ref, dm_ref, dwm_ref, dwp_ref):
        i = pl.program_id(0)
        dxa, dwm = _rms_bwd(x1_ref[...], wm_ref[...], dh2_ref[...])
        dx1 = dx2_ref[...] + dxa
        dx1_ref[...] = dx1
        dm, dwp = _rms_bwd(m_ref[...], wp_ref[...], dx1)
        dm_ref[...] = dm.astype(BF16)

        @pl.when(i == 0)
        def _():
            dwm_ref[...] = jnp.zeros_like(dwm_ref)
            dwp_ref[...] = jnp.zeros_like(dwp_ref)

        dwm_ref[...] += jnp.sum(dwm, axis=0, keepdims=True)
        dwp_ref[...] += jnp.sum(dwp, axis=0, keepdims=True)

    return pl.pallas_call(
        body, name="mid_bwd", grid=(S // TR,),
        in_specs=[_row_spec(D), _row_spec(D), _vec_spec(D), _row_spec(D), _row_spec(D), _vec_spec(D)],
        out_specs=[_row_spec(D), _row_spec(D), _vec_spec(D), _vec_spec(D)],
        out_shape=[jax.ShapeDtypeStruct((S, D), F32), jax.ShapeDtypeStruct((S, D), BF16),
                   jax.ShapeDtypeStruct((1, D), F32), jax.ShapeDtypeStruct((1, D), F32)],
        compiler_params=_cparams(),
    )(dh2, x1, w_pre_mlp, dx2, mixed, w_post)


def _pre_norm_bwd(dh, x, w, dx1):
    def body(dh_ref, x_ref, w_ref, dx1_ref, dx_ref, dw_ref):
        i = pl.program_id(0)
        dxa, dwt = _rms_bwd(x_ref[...], w_ref[...], dh_ref[...])
        dx_ref[...] = dx1_ref[...] + dxa

        @pl.when(i == 0)
        def _():
            dw_ref[...] = jnp.zeros_like(dw_ref)

        dw_ref[...] += jnp.sum(dwt, axis=0, keepdims=True)

    return pl.pallas_call(
        body, name="pre_norm_bwd", grid=(S // TR,),
        in_specs=[_row_spec(D), _row_spec(D), _vec_spec(D), _row_spec(D)], out_specs=[_row_spec(D), _vec_spec(D)],
        out_shape=[jax.ShapeDtypeStruct((S, D), F32), jax.ShapeDtypeStruct((1, D), F32)], compiler_params=_cparams(),
    )(dh, x, w, dx1)


BQ = 512
NQ = S // BQ
LANE_BETA, LANE_G = 8, 12


def _gate_lanes(shape):
    lane = lax.broadcasted_iota(jnp.int32, shape, 1)
    return lane < LANE_BETA, (lane >= LANE_BETA) & (lane < LANE_G), (lane >= LANE_G) & (lane < LANE_G + NGH)


def _gates(proj, bias_vec, alog_vec):
    def body(s_ref, b_ref, a_ref, o_ref, carry_ref):
        i = pl.program_id(0)

        @pl.when(i == 0)
        def _():
            carry_ref[...] = jnp.zeros_like(carry_ref)

        z = s_ref[...] + b_ref[...]
        tail = jnp.log(1.0 + jnp.exp(-jnp.abs(z)))
        sp = jnp.maximum(z, 0.0) + tail
        lf = jnp.minimum(z, 0.0) - tail
        r = lax.broadcasted_iota(jnp.int32, (BQ, BQ), 0)
        c = lax.broadcasted_iota(jnp.int32, (BQ, BQ), 1)
        tri = (c <= r).astype(F32)
        cum = _hdot(tri, lf) + carry_ref[...]
        carry_ref[...] = cum[BQ - 1:BQ, :]
        is_fox, is_beta, is_g = _gate_lanes(z.shape)
        o_ref[...] = jnp.where(is_fox, cum, jnp.where(is_beta, _sigmoid(z), jnp.where(is_g, -jnp.exp(a_ref[...]) * sp, 0.0)))

    return pl.pallas_call(
        body, name="gates", grid=(NQ,),
        in_specs=[pl.BlockSpec((BQ, LANES), lambda i: (i, BLK_SMALL)), _vec_spec(LANES), _vec_spec(LANES)],
        out_specs=pl.BlockSpec((BQ, LANES), lambda i: (i, 0)), out_shape=jax.ShapeDtypeStruct((S, LANES), F32),
        scratch_shapes=[pltpu.VMEM((1, LANES), F32)], compiler_params=_cparams(),
    )(proj, bias_vec, alog_vec)


def _gates_bwd(proj, bias_vec, alog_vec, dgates_gdn, dcum_fox, dproj):
    def body(s_ref, b_ref, a_ref, dg_ref, dc_ref, dproj_in, dproj_ref, red_ref, carry_ref):
        del dproj_in
        i = pl.program_id(0)

        @pl.when(i == 0)
        def _():
            carry_ref[...] = jnp.zeros_like(carry_ref)
            red_ref[...] = jnp.zeros_like(red_ref)

        z = s_ref[...] + b_ref[...]
        dg = dg_ref[...] + dc_ref[...]
        r = lax.broadcasted_iota(jnp.int32, (BQ, BQ), 0)
        c = lax.broadcasted_iota(jnp.int32, (BQ, BQ), 1)
        upper = (c >= r).astype(F32)
        dlf = _hdot(upper, dg) + carry_ref[...]
        carry_ref[...] = dlf[0:1, :]
        sig = _sigmoid(z)
        g_scale = -jnp.exp(a_ref[...])
        is_fox, is_beta, is_g = _gate_lanes(z.shape)
        ds = jnp.where(is_fox, dlf * (1.0 - sig), jnp.where(is_beta, dg * sig * (1.0 - sig), jnp.where(is_g, dg * g_scale * sig, 0.0)))
        dproj_ref[:, 0:LANES] = ds.astype(BF16)
        dproj_ref[:, LANES:2 * LANES] = jnp.zeros((BQ, LANES), BF16)
        dalog = jnp.where(is_g, dg * g_scale * _softplus(z), 0.0)
        sums = jnp.sum(ds, axis=0, keepdims=True)
        red_ref[0:1, :] += jnp.where(is_fox[0:1], sums, 0.0)
        red_ref[1:2, :] += pltpu.roll(jnp.where(is_g[0:1], sums, 0.0), LANES - LANE_G, 1)
        red_ref[2:3, :] += pltpu.roll(jnp.sum(dalog, axis=0, keepdims=True), LANES - LANE_G, 1)

    blk = pl.BlockSpec((BQ, LANES), lambda i: (NQ - 1 - i, 0))
    return pl.pallas_call(
        body, name="gates_bwd", grid=(NQ,),
        in_specs=[pl.BlockSpec((BQ, LANES), lambda i: (NQ - 1 - i, BLK_SMALL)), _vec_spec(LANES), _vec_spec(LANES), blk, blk,
                  pl.BlockSpec(memory_space=pl.ANY)],
        out_specs=[pl.BlockSpec((BQ, 2 * LANES), lambda i: (NQ - 1 - i, BLK_SMALL // 2)), pl.BlockSpec((8, LANES), lambda i: (0, 0))],
        out_shape=[jax.ShapeDtypeStruct((S, DPROJ_PAD), BF16), jax.ShapeDtypeStruct((8, LANES), F32)],
        input_output_aliases={5: 0},
        scratch_shapes=[pltpu.VMEM((1, LANES), F32)], compiler_params=_cparams(),
    )(proj, bias_vec, alog_vec, dgates_gdn, dcum_fox, dproj)


FOX_SCALE = FHD ** -0.5
FOX_PAIRS = 2
FOX_PAIRS_BWD = 2


def _head_mask(e):
    lane = lax.broadcasted_iota(jnp.int32, (1, LANES), 1)
    return (lane >= e * FHD) & (lane < (e + 1) * FHD)


def _lane_col(vals, index):
    lane = lax.broadcasted_iota(jnp.int32, vals.shape, 1)
    return jnp.sum(jnp.where(lane == index, vals, 0.0), axis=1, keepdims=True)


def _sublane_row(vals, index):
    row = lax.broadcasted_iota(jnp.int32, vals.shape, 0)
    return jnp.sum(jnp.where(row == index, vals, 0.0), axis=0, keepdims=True)


def _pair_cols(c0, c1):
    lane = lax.broadcasted_iota(jnp.int32, (c0.shape[0], 2), 1)
    return jnp.where(lane == 0, c0, c1)


def _split3(x):
    hi = x.astype(BF16).astype(F32)
    rest = x - hi
    mid = rest.astype(BF16).astype(F32)
    return hi, mid, (rest - mid).astype(BF16).astype(F32)


def _fox_operand(vals, e, cum, is_query):
    lane = lax.broadcasted_iota(jnp.int32, (1, LANES), 1)
    base = (1 - e) * FHD
    parts = _split3(cum)
    own = jnp.where(_head_mask(e), vals * FOX_SCALE if is_query else vals, 0.0)
    cum_at, ones_at = (base, base + 3) if is_query else (base + 3, base)
    sign = 1.0 if is_query else -1.0
    out = own + jnp.where((lane >= ones_at) & (lane < ones_at + 3), 1.0, 0.0)
    for i, part in enumerate(parts):
        out = out + jnp.where(lane == cum_at + i, sign * part, 0.0)
    return out.astype(BF16)


def _causal_block():
    return lax.broadcasted_iota(jnp.int32, (BQ, BQ), 1) <= lax.broadcasted_iota(jnp.int32, (BQ, BQ), 0)


def _head_rms(o, masks):
    o2 = o * o
    r = [lax.rsqrt(jnp.sum(jnp.where(mk, o2, 0.0), axis=1, keepdims=True) * (1.0 / FHD) + EPS) for mk in masks]
    return jnp.where(masks[0], r[0], r[1])


def _hosted(exchange):
    if exchange is None:
        return [], [], [], [], []
    return (exchange.inputs, [HBM] * len(exchange.inputs), [HBM] * len(exchange.out_shape), exchange.out_shape,
            exchange.sem_shapes())


def _fox_fwd(proj, gates, w2, exchange=None):
    ex_in, ex_in_specs, ex_out_specs, ex_out_shape, ex_scratch = _hosted(exchange)

    n_in = 3 * FOX_PAIRS + 2
    heads = [(pp, e) for pp in range(FOX_PAIRS) for e in range(2)]

    def body(*refs):
        qkv_refs, g_ref, w_ref = refs[:3 * FOX_PAIRS], refs[3 * FOX_PAIRS], refs[3 * FOX_PAIRS + 1]
        mix_ref, o_ref, lse_ref = refs[n_in + len(ex_in):n_in + 3 + len(ex_in)]
        ka_ref, vb_ref = refs[n_in + 3 + len(ex_in) + len(ex_out_shape):n_in + 5 + len(ex_in) + len(ex_out_shape)]
        ex_refs = refs[n_in:n_in + len(ex_in)] + refs[n_in + 3 + len(ex_in):n_in + 3 + len(ex_in) + len(ex_out_shape)] + refs[-2:]
        grp, qi = pl.program_id(0), pl.program_id(1)

        def head_index(pp, e):
            return 2 * (FOX_PAIRS * grp + pp) + e

        if exchange is not None:
            @pl.when((grp == 0) & (qi == 0))
            def _():
                exchange.start(*exchange.split(ex_refs))

        @pl.when(qi == 0)
        def _():
            gt = g_ref[...]
            for pp in range(FOX_PAIRS):
                kv = qkv_refs[3 * pp + 1][...]
                for e in range(2):
                    ka_ref[2 * pp + e] = _fox_operand(kv, e, _lane_col(gt, head_index(pp, e)), False)
                vb_ref[pp] = qkv_refs[3 * pp + 2][...].astype(BF16)

        masks = [_head_mask(0), _head_mask(1)]
        gt = g_ref[pl.ds(pl.multiple_of(qi * BQ, BQ), BQ), :]
        qs = [_fox_operand(qkv_refs[3 * pp][...], e, _lane_col(gt, head_index(pp, e)), True) for pp, e in heads]
        n = range(len(heads))

        def block(kj, carry, diagonal):
            rows = pl.ds(pl.multiple_of(kj * BQ, BQ), BQ)
            s = [_dot(qs[i], ka_ref[i, rows, :], 1, 1) for i in n]
            if diagonal:
                s = [jnp.where(_causal_block(), s[i], -jnp.inf) for i in n]
            m_new = [jnp.maximum(carry[i][0], jnp.max(s[i], axis=-1, keepdims=True)) for i in n]
            p = [jnp.exp(s[i] - m_new[i]) for i in n]
            alpha = [jnp.exp(carry[i][0] - m_new[i]) for i in n]
            l_new = [alpha[i] * carry[i][1] + jnp.sum(p[i], axis=-1, keepdims=True) for i in n]
            pv = [_dot(p[i], vb_ref[heads[i][0], rows, :]) for i in n]
            return tuple((m_new[i], l_new[i], alpha[i] * carry[i][2] + pv[i]) for i in n)

        one = (jnp.full((BQ, 1), -jnp.inf, F32), jnp.zeros((BQ, 1), F32), jnp.zeros((BQ, LANES), F32))
        below = lax.fori_loop(0, qi, lambda kj, carry: block(kj, carry, False), (one,) * len(heads))
        done = block(qi, below, True)
        for pp in range(FOX_PAIRS):
            (m0, l0, a0), (m1, l1, a1) = done[2 * pp], done[2 * pp + 1]
            o = jnp.where(masks[0], a0 / l0, a1 / l1)
            cols = slice(pp * LANES, (pp + 1) * LANES)
            o_ref[:, cols] = o
            mix_ref[:, cols] = (o * _head_rms(o, masks) * w_ref[...]).astype(BF16)
            lse_ref[pp] = _pair_cols(m0 + jnp.log(l0), m1 + jnp.log(l1))

        if exchange is not None:
            @pl.when((grp == NPAIR // FOX_PAIRS // 2) & (qi == 0))
            def _():
                exchange.middle(*exchange.split(ex_refs))

            @pl.when((grp == NPAIR // FOX_PAIRS - 1) & (qi == NQ - 1))
            def _():
                exchange.rest(*exchange.split(ex_refs))

    qkv_specs = []
    for pp in range(FOX_PAIRS):
        qkv_specs.append(pl.BlockSpec((BQ, LANES), lambda g, i, pp=pp: (i, 3 * (FOX_PAIRS * g + pp))))
        qkv_specs.append(pl.BlockSpec((S, LANES), lambda g, i, pp=pp: (0, 3 * (FOX_PAIRS * g + pp) + 1)))
        qkv_specs.append(pl.BlockSpec((S, LANES), lambda g, i, pp=pp: (0, 3 * (FOX_PAIRS * g + pp) + 2)))
    blk = pl.BlockSpec((BQ, FOX_PAIRS * LANES), lambda g, i: (i, g))
    res = pl.pallas_call(
        body, name="fox_fwd", grid=(NPAIR // FOX_PAIRS, NQ),
        in_specs=qkv_specs + [pl.BlockSpec((S, LANES), lambda g, i: (0, 0)), pl.BlockSpec((1, LANES), lambda g, i: (0, 0))]
        + ex_in_specs,
        out_specs=[blk, blk, pl.BlockSpec((FOX_PAIRS, BQ, 2), lambda g, i: (g, i, 0))] + ex_out_specs,
        out_shape=[jax.ShapeDtypeStruct((S, D), BF16), jax.ShapeDtypeStruct((S, DFOX), F32),
                   jax.ShapeDtypeStruct((NPAIR, S, 2), F32)] + ex_out_shape,
        scratch_shapes=[pltpu.VMEM((2 * FOX_PAIRS, S, LANES), BF16), pltpu.VMEM((FOX_PAIRS, S, LANES), BF16)] + ex_scratch,
        compiler_params=_cparams(),
    )(*([proj] * (3 * FOX_PAIRS)), gates, w2, *ex_in)
    return res[0], res[1], res[2], res[3:]


def _fox_norm_bwd(o, dmix, w2, exchange=None):
    ex_in, ex_in_specs, ex_out_specs, ex_out_shape, ex_scratch = _hosted(exchange)

    def body(*refs):
        o_ref, g_ref, w_ref = refs[:3]
        do_ref, dl_ref, dw_ref = refs[3 + len(ex_in):6 + len(ex_in)]
        ex_refs = refs[3:3 + len(ex_in)] + refs[6 + len(ex_in):]
        hp, qi = pl.program_id(0), pl.program_id(1)

        if exchange is not None:
            @pl.when((hp == 0) & (qi == 0))
            def _():
                exchange.start(*exchange.split(ex_refs))

        masks = [_head_mask(0), _head_mask(1)]
        ov = o_ref[...]
        g = g_ref[...]
        r = _head_rms(ov, masks)
        gw = g * w_ref[...]
        gwo = gw * ov
        mean = [jnp.sum(jnp.where(mk, gwo, 0.0), axis=1, keepdims=True) * (1.0 / FHD) for mk in masks]
        do = r * gw - ov * (r * r * r) * jnp.where(masks[0], mean[0], mean[1])
        do_ref[...] = do.astype(BF16)
        doo = do * ov
        dl_ref[...] = _pair_cols(*[jnp.sum(jnp.where(mk, doo, 0.0), axis=1, keepdims=True) for mk in masks])

        @pl.when((hp == 0) & (qi == 0))
        def _():
            dw_ref[...] = jnp.zeros_like(dw_ref)

        dw_ref[...] += jnp.sum(g * ov * r, axis=0, keepdims=True)

        @pl.when((hp == NPAIR - 1) & (qi == NQ - 1))
        def _():
            dw = dw_ref[...]
            dw_ref[...] = dw + pltpu.roll(dw, FHD, 1)
            if exchange is not None:
                exchange.finish(*exchange.split(ex_refs))

    blk = pl.BlockSpec((BQ, LANES), lambda hp, i: (i, hp))
    vec = pl.BlockSpec((1, LANES), lambda hp, i: (0, 0))
    res = pl.pallas_call(
        body, name="fox_norm_bwd", grid=(NPAIR, NQ), in_specs=[blk, blk, vec] + ex_in_specs,
        out_specs=[blk, pl.BlockSpec((None, BQ, 2), lambda hp, i: (hp, i, 0)), vec] + ex_out_specs,
        out_shape=[jax.ShapeDtypeStruct((S, DFOX), BF16), jax.ShapeDtypeStruct((NPAIR, S, 2), F32),
                   jax.ShapeDtypeStruct((1, LANES), F32)] + ex_out_shape,
        scratch_shapes=ex_scratch, compiler_params=_cparams(),
    )(o, dmix, w2, *ex_in)
    return res[0], res[1], res[2], res[3:]


def _fox_bwd(proj, do, gates, lse, delta, exchange=None):
    ex_in, ex_in_specs, ex_out_specs, ex_out_shape, ex_scratch = _hosted(exchange)

    pg = FOX_PAIRS_BWD
    n_in = 3 * pg + 4
    heads = [(pp, e) for pp in range(pg) for e in range(2)]

    def body(*refs):
        qkv_refs = refs[:3 * pg]
        do_ref, g_ref, lse_ref, dl_ref = refs[3 * pg:n_in]
        dproj_ref, dc_ref = refs[n_in + len(ex_in):n_in + 2 + len(ex_in)]
        qa_ref, dq_ref = refs[n_in + 2 + len(ex_in) + len(ex_out_shape):n_in + 4 + len(ex_in) + len(ex_out_shape)]
        ex_refs = refs[n_in:n_in + len(ex_in)] + refs[n_in + 2 + len(ex_in):n_in + 2 + len(ex_in) + len(ex_out_shape)] + refs[-2:]
        grp, kj = pl.program_id(0), pl.program_id(1)

        def head_index(pp, e):
            return 2 * (pg * grp + pp) + e

        if exchange is not None:
            @pl.when((grp == 0) & (kj == 0))
            def _():
                exchange.start(*exchange.split(ex_refs))

        @pl.when(kj == 0)
        def _():
            gt = g_ref[...]
            for pp in range(pg):
                qv = qkv_refs[3 * pp][...]
                for e in range(2):
                    qa_ref[2 * pp + e] = _fox_operand(qv, e, _lane_col(gt, head_index(pp, e)), True)
            dq_ref[...] = jnp.zeros_like(dq_ref)

        @pl.when((grp == 0) & (kj == 0))
        def _():
            dc_ref[...] = jnp.zeros_like(dc_ref)

        masks = [_head_mask(0), _head_mask(1)]
        krows = pl.ds(pl.multiple_of(kj * BQ, BQ), BQ)
        gk = g_ref[krows, :]
        kas = [_fox_operand(qkv_refs[3 * pp + 1][...], e, _lane_col(gk, head_index(pp, e)), False) for pp, e in heads]
        vbs = [qkv_refs[3 * pp + 2][...].astype(BF16) for pp in range(pg)]
        lane = lax.broadcasted_iota(jnp.int32, (BQ, LANES), 1)
        n = range(len(heads))

        def block(qi, carry, diagonal):
            dks, dvs, css = carry
            rows = pl.ds(pl.multiple_of(qi * BQ, BQ), BQ)
            qa = [qa_ref[i, rows, :] for i in n]
            s = [_dot(qa[i], kas[i], 1, 1) for i in n]
            if diagonal:
                s = [jnp.where(_causal_block(), s[i], -jnp.inf) for i in n]
            dov = [do_ref[rows, pp * LANES:(pp + 1) * LANES] for pp in range(pg)]
            doe = [jnp.where(masks[e], dov[pp], jnp.zeros_like(dov[pp])) for pp, e in heads]
            lse2 = [lse_ref[pp, rows, :] for pp in range(pg)]
            dl2 = [dl_ref[pp, rows, :] for pp in range(pg)]
            p = [jnp.exp(s[i] - _lane_col(lse2[heads[i][0]], heads[i][1])) for i in n]
            dp = [_dot(doe[i], vbs[heads[i][0]], 1, 1) for i in n]
            ds = [p[i] * (dp[i] - _lane_col(dl2[heads[i][0]], heads[i][1])) for i in n]
            dv_part = [_dot(p[i], doe[i], 0, 0) for i in n]
            dk_part = [_dot(ds[i], jnp.where(masks[heads[i][1]], qa[i], jnp.zeros_like(qa[i])), 0, 0) for i in n]
            dq_part = [jnp.where(masks[heads[i][1]], _dot(ds[i], kas[i]), 0.0) for i in n]
            css = tuple(css[i] + jnp.sum(ds[i], axis=0, keepdims=True) for i in n)
            dc = jnp.zeros((BQ, LANES), F32)
            for i in n:
                dc = dc + jnp.where(lane == head_index(*heads[i]), jnp.sum(ds[i], axis=1, keepdims=True), 0.0)
            for pp in range(pg):
                dq_ref[pp, rows, :] += (dq_part[2 * pp] + dq_part[2 * pp + 1]) * FOX_SCALE
            dc_ref[rows, :] += dc
            dks = tuple(dks[pp] + dk_part[2 * pp] + dk_part[2 * pp + 1] for pp in range(pg))
            dvs = tuple(dvs[pp] + dv_part[2 * pp] + dv_part[2 * pp + 1] for pp in range(pg))
            return dks, dvs, css

        zero = jnp.zeros((BQ, LANES), F32)
        first = block(kj, ((zero,) * pg, (zero,) * pg, (jnp.zeros((1, BQ), F32),) * len(heads)), True)
        dks, dvs, css = lax.fori_loop(kj + 1, NQ, lambda qi, carry: block(qi, carry, False), first)
        r = lax.broadcasted_iota(jnp.int32, (BQ, BQ), 0)
        c = lax.broadcasted_iota(jnp.int32, (BQ, BQ), 1)
        dcol = jnp.zeros((BQ, LANES), F32)
        for i in n:
            col = jnp.sum(jnp.where(r == c, css[i], 0.0), axis=1, keepdims=True)
            dcol = dcol + jnp.where(lane == head_index(*heads[i]), col, 0.0)
        dc_ref[krows, :] -= dcol
        for pp in range(pg):
            base = 3 * pp * LANES
            dproj_ref[krows, base + LANES:base + 2 * LANES] = dks[pp].astype(BF16)
            dproj_ref[krows, base + 2 * LANES:base + 3 * LANES] = dvs[pp].astype(BF16)

        @pl.when(kj == NQ - 1)
        def _():
            for pp in range(pg):
                dproj_ref[:, 3 * pp * LANES:(3 * pp + 1) * LANES] = dq_ref[pp].astype(BF16)

        if exchange is not None:
            @pl.when((grp == NPAIR // pg // 2) & (kj == 0))
            def _():
                exchange.middle(*exchange.split(ex_refs))

            @pl.when((grp == NPAIR // pg - 1) & (kj == NQ - 1))
            def _():
                exchange.rest(*exchange.split(ex_refs))

    qkv_specs = []
    for pp in range(pg):
        qkv_specs.append(pl.BlockSpec((S, LANES), lambda g, j, pp=pp: (0, 3 * (pg * g + pp))))
        qkv_specs.append(pl.BlockSpec((BQ, LANES), lambda g, j, pp=pp: (j, 3 * (pg * g + pp) + 1)))
        qkv_specs.append(pl.BlockSpec((BQ, LANES), lambda g, j, pp=pp: (j, 3 * (pg * g + pp) + 2)))
    pair = pl.BlockSpec((pg, S, 2), lambda g, j: (g, 0, 0))
    res = pl.pallas_call(
        body, name="fox_bwd", grid=(NPAIR // pg, NQ),
        in_specs=qkv_specs + [pl.BlockSpec((S, pg * LANES), lambda g, j: (0, g)), pl.BlockSpec((S, LANES), lambda g, j: (0, 0)),
                              pair, pair] + ex_in_specs,
        out_specs=[pl.BlockSpec((S, 3 * pg * LANES), lambda g, j: (0, g)), pl.BlockSpec((S, LANES), lambda g, j: (0, 0))]
        + ex_out_specs,
        out_shape=[jax.ShapeDtypeStruct((S, DPROJ_PAD), BF16), jax.ShapeDtypeStruct((S, LANES), F32)] + ex_out_shape,
        scratch_shapes=[pltpu.VMEM((2 * pg, S, LANES), BF16), pltpu.VMEM((pg, S, LANES), F32)] + ex_scratch,
        compiler_params=_cparams(),
    )(*([proj] * (3 * pg)), do, gates, lse, delta, *ex_in)
    return res[0], res[1], res[2:]


NQKV = 3 * NGH
GDN_QSCALE = GHD ** -0.5


def _shift_down(x, s):
    if s == 0:
        return x
    row = lax.broadcasted_iota(jnp.int32, x.shape, 0)
    return jnp.where(row >= s, pltpu.roll(x, s, 0), 0.0)


def _shift_up(x, s):
    if s == 0:
        return x
    n = x.shape[0]
    row = lax.broadcasted_iota(jnp.int32, x.shape, 0)
    return jnp.where(row < n - s, pltpu.roll(x, n - s, 0), 0.0)


def _conv_pre(xv, wv):
    pre = xv * wv[CONV_K - 1:CONV_K, :]
    for j in range(CONV_K - 1):
        pre = pre + _shift_down(xv, CONV_K - 1 - j) * wv[j:j + 1, :]
    return pre


def _l2_factors(b):
    return b < 2 * NGH, jnp.where(b < NGH, GDN_QSCALE, 1.0)


def _gdn_pre(proj, conv_w):
    def body(x_ref, w_ref, o_ref):
        b = pl.program_id(0)
        c = _silu(_conv_pre(x_ref[...], w_ref[...]))
        normed, scale = _l2_factors(b)
        rs = lax.rsqrt(jnp.sum(c * c, axis=-1, keepdims=True) + EPS)
        o_ref[...] = c * jnp.where(normed, rs, 1.0) * scale

    return pl.pallas_call(
        body, name="gdn_pre", grid=(NQKV,),
        in_specs=[pl.BlockSpec((S, GHD), lambda b: (0, BLK_GDN + b)), pl.BlockSpec((CONV_K, GHD), lambda b: (0, b))],
        out_specs=pl.BlockSpec((S, GHD), lambda b: (0, b)),
        out_shape=jax.ShapeDtypeStruct((S, NQKV * GHD), F32), compiler_params=_cparams(),
    )(proj, conv_w)


def _gdn_pre_bwd(proj, conv_w, dqkv, dproj):
    def body(x_ref, w_ref, dy_ref, dproj_in, dx_ref, dw_ref):
        del dproj_in
        b = pl.program_id(0)
        xv = x_ref[...]
        wv = w_ref[...]
        pre = _conv_pre(xv, wv)
        sig = _sigmoid(pre)
        c = pre * sig
        normed, scale = _l2_factors(b)
        g = dy_ref[...] * scale
        rs = lax.rsqrt(jnp.sum(c * c, axis=-1, keepdims=True) + EPS)
        dc_n = rs * g - c * (rs * rs * rs) * jnp.sum(g * c, axis=-1, keepdims=True)
        dc = jnp.where(normed, dc_n, g)
        dpre = dc * sig * (1.0 + pre * (1.0 - sig))
        dx = dpre * wv[CONV_K - 1:CONV_K, :]
        for j in range(CONV_K - 1):
            dx = dx + _shift_up(dpre, CONV_K - 1 - j) * wv[j:j + 1, :]
        dx_ref[...] = dx.astype(BF16)
        for j in range(CONV_K):
            dw_ref[j:j + 1, :] = jnp.sum(dpre * _shift_down(xv, CONV_K - 1 - j), axis=0, keepdims=True)

    return pl.pallas_call(
        body, name="gdn_pre_bwd", grid=(NQKV,),
        in_specs=[pl.BlockSpec((S, GHD), lambda b: (0, BLK_GDN + b)), pl.BlockSpec((CONV_K, GHD), lambda b: (0, b)),
                  pl.BlockSpec((None, S, GHD), lambda b: (b // NGH, 0, b % NGH)), pl.BlockSpec(memory_space=pl.ANY)],
        out_specs=[pl.BlockSpec((S, GHD), lambda b: (0, BLK_GDN + b)), pl.BlockSpec((CONV_K, GHD), lambda b: (0, b))],
        out_shape=[jax.ShapeDtypeStruct((S, DPROJ_PAD), BF16), jax.ShapeDtypeStruct((CONV_K, NQKV * GHD), F32)],
        input_output_aliases={3: 0}, compiler_params=_cparams(),
    )(proj, conv_w, dqkv, dproj)


CB = 16
NCB = NCH // CB


def _chunk_prep(qs, ks, vs, gcols, bcols, t_saved=None):
    n = range(len(qs))
    r = lax.broadcasted_iota(jnp.int32, (CHUNK, CHUNK), 0)
    c = lax.broadcasted_iota(jnp.int32, (CHUNK, CHUNK), 1)
    incl = c <= r
    eye = (r == c).astype(F32)
    grow = [jnp.sum(gcols[i] * eye, axis=0, keepdims=True) for i in n]
    gc_col = [jnp.sum(jnp.where(incl, grow[i], 0.0), axis=1, keepdims=True) for i in n]
    gc_row = [jnp.sum(jnp.where(r <= c, gcols[i], 0.0), axis=0, keepdims=True) for i in n]
    decay = [jnp.exp(jnp.where(incl, gc_col[i] - gc_row[i], -jnp.inf)) for i in n]
    kb = [ks[i] * bcols[i] for i in n]
    vb = [vs[i] * bcols[i] for i in n]
    kk = [_mm_nt(kb[i], ks[i]) for i in n]
    m = [jnp.where(c < r, kk[i] * decay[i], 0.0) for i in n]
    if t_saved is None:
        t_inv = [eye - m[i] for i in n]
        p = [_dot3(m[i], m[i]) for i in n]
        for step in range(5):
            t_inv = [t_inv[i] + _dot3(t_inv[i], p[i]) for i in n]
            if step < 4:
                p = [_dot3(p[i], p[i]) for i in n]
    else:
        t_inv = [_saved_inverse(m[i], t_saved[i]) for i in n]
    egc = [jnp.exp(gc_col[i]) for i in n]
    u = [_mm_nn(t_inv[i], vb[i]) for i in n]
    w = [_mm_nn(t_inv[i], kb[i] * egc[i]) for i in n]
    qk = [_mm_nt(qs[i], ks[i]) for i in n]
    gc_last = [gc_col[i][CHUNK - 1:CHUNK, :] for i in n]
    return [(u[i], w[i], qk[i] * decay[i], qs[i] * egc[i], ks[i] * jnp.exp(gc_last[i] - gc_col[i]), jnp.exp(gc_last[i]),
             t_inv[i]) for i in n]


def _prep_specs():
    rows = CB * CHUNK
    qs = pl.BlockSpec((rows, GHD), lambda i, h: (i, h))
    ks = pl.BlockSpec((rows, GHD), lambda i, h: (i, NGH + h))
    vs = pl.BlockSpec((rows, GHD), lambda i, h: (i, 2 * NGH + h))
    gs = pl.BlockSpec((rows, LANES), lambda i, h: (i, 0))
    a_s = pl.BlockSpec((None, rows, CHUNK), lambda i, h: (h, i, 0))
    gl_s = pl.BlockSpec((None, CB, 1, LANES), lambda i, h: (h, i, 0, 0))
    return qs, ks, vs, gs, a_s, gl_s


def _gdn_prep(qkv, gates, exchange=None):
    ex_in, ex_in_specs, ex_out_specs, ex_out_shape, ex_scratch = _hosted(exchange)

    def body(*refs):
        q_ref, k_ref, v_ref, g_ref = refs[:4]
        u_ref, w_ref, qd_ref, kd_ref, a_ref, gl_ref, t_ref = refs[4 + len(ex_in):11 + len(ex_in)]
        ex_refs = refs[4:4 + len(ex_in)] + refs[11 + len(ex_in):]
        h = pl.program_id(1)

        if exchange is not None:
            @pl.when((pl.program_id(0) == 0) & (h == 0))
            def _():
                exchange.start(*exchange.split(ex_refs))

        chunks = [pl.ds(cidx * CHUNK, CHUNK) for cidx in range(CB)]
        gts = [g_ref[rows, :] for rows in chunks]
        outs = _chunk_prep([q_ref[rows, :] for rows in chunks], [k_ref[rows, :] for rows in chunks],
                           [v_ref[rows, :] for rows in chunks], [_lane_col(gt, LANE_G + h) for gt in gts],
                           [_lane_col(gt, LANE_BETA + h) for gt in gts])
        for cidx, rows in enumerate(chunks):
            u, w, a, qd, kd, gl, t_inv = outs[cidx]
            u_ref[rows, :] = u
            w_ref[rows, :] = w
            qd_ref[rows, :] = qd
            kd_ref[rows, :] = kd
            a_ref[rows, :] = a
            t_ref[rows, :] = t_inv
            gl_ref[cidx] = jnp.broadcast_to(gl, (1, LANES))

        if exchange is not None:
            @pl.when((pl.program_id(0) == NCB // 2) & (h == 0))
            def _():
                exchange.middle(*exchange.split(ex_refs))

            @pl.when((pl.program_id(0) == NCB - 1) & (h == NGH - 1))
            def _():
                exchange.rest(*exchange.split(ex_refs))

    qs, ks, vs, gs, a_s, gl_s = _prep_specs()
    tok = jax.ShapeDtypeStruct((S, DGDN), F32)
    sq = jax.ShapeDtypeStruct((NGH, S, CHUNK), F32)
    res = pl.pallas_call(
        body, name="gdn_prep", grid=(NCB, NGH), in_specs=[qs, ks, vs, gs] + ex_in_specs,
        out_specs=[qs, qs, qs, qs, a_s, gl_s, a_s] + ex_out_specs,
        out_shape=[tok, tok, tok, tok, sq, jax.ShapeDtypeStruct((NGH, NCH, 1, LANES), F32), sq] + ex_out_shape,
        scratch_shapes=ex_scratch, compiler_params=_cparams(),
    )(qkv, qkv, qkv, gates, *ex_in)
    return res[:7], res[7:]


def _gdn_prep_bwd(qkv, gates, t_inv, du, dw, dqd, dkd, da, dgl, exchange=None):
    ex_in, ex_in_specs, ex_out_specs, ex_out_shape, ex_scratch = _hosted(exchange)

    def body(*refs):
        q_ref, k_ref, v_ref, g_ref, t_ref, du_ref, dw_ref, dqd_ref, dkd_ref, da_ref, dgl_ref = refs[:11]
        dqkv_ref, dg_ref = refs[11 + len(ex_in):13 + len(ex_in)]
        ex_refs = refs[11:11 + len(ex_in)] + refs[13 + len(ex_in):]
        h = pl.program_id(1)

        if exchange is not None:
            @pl.when((pl.program_id(0) == 0) & (h == 0))
            def _():
                exchange.start(*exchange.split(ex_refs))

        @pl.when(h == 0)
        def _():
            dg_ref[...] = jnp.zeros_like(dg_ref)

        lane = lax.broadcasted_iota(jnp.int32, (CHUNK, LANES), 1)
        chunks = [pl.ds(cidx * CHUNK, CHUNK) for cidx in range(CB)]
        gts = [g_ref[rows, :] for rows in chunks]
        t_saved = [t_ref[rows, :] for rows in chunks]
        _, vjp = jax.vjp(lambda *args: [o[:6] for o in _chunk_prep(*args, t_saved=t_saved)],
                         [q_ref[rows, :] for rows in chunks], [k_ref[rows, :] for rows in chunks],
                         [v_ref[rows, :] for rows in chunks], [_lane_col(gt, LANE_G + h) for gt in gts],
                         [_lane_col(gt, LANE_BETA + h) for gt in gts])
        dqs, dks, dvs, dgcs, dbcs = vjp([(du_ref[rows, :], dw_ref[rows, :], da_ref[rows, :], dqd_ref[rows, :],
                                          dkd_ref[rows, :], dgl_ref[cidx][:, 0:1]) for cidx, rows in enumerate(chunks)])
        for cidx, rows in enumerate(chunks):
            dq, dk, dv, dgc, dbc = dqs[cidx], dks[cidx], dvs[cidx], dgcs[cidx], dbcs[cidx]
            dqkv_ref[0, rows, :] = dq
            dqkv_ref[1, rows, :] = dk
            dqkv_ref[2, rows, :] = dv
            dg_ref[rows, :] += jnp.where(lane == LANE_G + h, dgc, 0.0) + jnp.where(lane == LANE_BETA + h, dbc, 0.0)

        if exchange is not None:
            @pl.when((pl.program_id(0) == NCB - 1) & (h == NGH - 1))
            def _():
                exchange.finish(*exchange.split(ex_refs))

    qs, ks, vs, gs, a_s, gl_s = _prep_specs()
    res = pl.pallas_call(
        body, name="gdn_prep_bwd", grid=(NCB, NGH), in_specs=[qs, ks, vs, gs, a_s, qs, qs, qs, qs, a_s, gl_s] + ex_in_specs,
        out_specs=[pl.BlockSpec((3, CB * CHUNK, GHD), lambda i, h: (0, i, h)), gs] + ex_out_specs,
        out_shape=[jax.ShapeDtypeStruct((3, S, DGDN), F32), jax.ShapeDtypeStruct((S, LANES), F32)] + ex_out_shape,
        scratch_shapes=ex_scratch, compiler_params=_cparams(),
    )(qkv, qkv, qkv, gates, t_inv, du, dw, dqd, dkd, da, dgl, *ex_in)
    return res[0], res[1], res[2:]


def _scan_specs(nh, parts, reverse):
    wide, rows, chunks = nh * GHD, S // parts, NCH // parts

    def part(p):
        return parts - 1 - p if reverse else p

    hs = pl.BlockSpec((rows, wide), lambda g, p: (part(p), g))
    a_s = pl.BlockSpec((nh, rows, CHUNK), lambda g, p: (g, part(p), 0))
    gl_s = pl.BlockSpec((nh, chunks, 1, LANES), lambda g, p: (g, part(p), 0, 0))
    st_s = pl.BlockSpec((nh, chunks, GHD, GHD), lambda g, p: (g, part(p), 0, 0))
    gz_s = pl.BlockSpec((rows, wide), lambda g, p: (part(p), BLK_GZ // nh + g))
    mix_s = pl.BlockSpec((rows, wide), lambda g, p: (part(p), NPAIR // nh + g))
    return hs, a_s, gl_s, st_s, gz_s, mix_s


def _head_cols(hh):
    return slice(hh * GHD, (hh + 1) * GHD)


SCAN_HEADS, SCAN_PARTS = 4, 2
SCAN_HEADS_BWD, SCAN_PARTS_BWD = 2, 2


def _gdn_scan(u, w, qd, kd, a, gl, proj, w_norm, mix):
    heads = range(SCAN_HEADS)

    def body(u_ref, w_ref, qd_ref, kd_ref, a_ref, gl_ref, z_ref, wn_ref, mix_in, mix_ref, o_ref, st_ref, carry_ref):
        del mix_in

        @pl.when(pl.program_id(1) == 0)
        def _():
            carry_ref[...] = jnp.zeros_like(carry_ref)

        def step(ci, states):
            rows = pl.ds(pl.multiple_of(ci * CHUNK, CHUNK), CHUNK)
            for hh in heads:
                st_ref[hh, ci] = states[hh]
            ws = [_dot(w_ref[rows, _head_cols(hh)], states[hh]) for hh in heads]
            qs = [_dot(qd_ref[rows, _head_cols(hh)], states[hh]) for hh in heads]
            vn = [u_ref[rows, _head_cols(hh)] - ws[hh] for hh in heads]
            av = [_dot(a_ref[hh, rows, :], vn[hh]) for hh in heads]
            kv = [_dot(kd_ref[rows, _head_cols(hh)], vn[hh], 0, 0) for hh in heads]
            for hh in heads:
                o_ref[rows, _head_cols(hh)] = qs[hh] + av[hh]
            return tuple(states[hh] * gl_ref[hh, ci] + kv[hh] for hh in heads)

        last = lax.fori_loop(0, NCH // SCAN_PARTS, step, tuple(carry_ref[hh] for hh in heads))
        for hh in heads:
            carry_ref[hh] = last[hh]
            ov = o_ref[:, _head_cols(hh)]
            mix_ref[:, _head_cols(hh)] = (ov * _rms_scale(ov) * wn_ref[...] * _silu(z_ref[:, _head_cols(hh)])).astype(BF16)

    hs, a_s, gl_s, st_s, gz_s, mix_s = _scan_specs(SCAN_HEADS, SCAN_PARTS, False)
    return pl.pallas_call(
        body, name="gdn_scan", grid=(NGH // SCAN_HEADS, SCAN_PARTS),
        in_specs=[hs, hs, hs, hs, a_s, gl_s, gz_s, pl.BlockSpec((1, GHD), lambda g, p: (0, 0)),
                  pl.BlockSpec(memory_space=pl.ANY)],
        out_specs=[mix_s, hs, st_s],
        out_shape=[jax.ShapeDtypeStruct((S, D), BF16), jax.ShapeDtypeStruct((S, DGDN), F32),
                   jax.ShapeDtypeStruct((NGH, NCH, GHD, GHD), F32)],
        input_output_aliases={8: 0}, scratch_shapes=[pltpu.VMEM((SCAN_HEADS, GHD, GHD), F32)], compiler_params=_cparams(),
    )(u, w, qd, kd, a, gl, proj, w_norm, mix)


def _gdn_scan_bwd(dmix, o, proj, w_norm, u, w, qd, kd, a, gl, states, dproj, exchange=None):
    ex_in, ex_in_specs, ex_out_specs, ex_out_shape, ex_scratch = _hosted(exchange)
    groups = NGH // SCAN_HEADS_BWD

    def body(*refs):
        dy_ref, o_ref, z_ref, wn_ref, u_ref, w_ref, qd_ref, kd_ref, a_ref, gl_ref, st_ref = refs[:11]
        dz_ref, du_ref, dw_ref, dqd_ref, dkd_ref, da_ref, dgl_ref, dwn_ref = refs[12 + len(ex_in):20 + len(ex_in)]
        do_ref, carry_ref = refs[20 + len(ex_in) + len(ex_out_shape):22 + len(ex_in) + len(ex_out_shape)]
        ex_refs = refs[12:12 + len(ex_in)] + refs[20 + len(ex_in):20 + len(ex_in) + len(ex_out_shape)] + refs[-2:]
        heads = range(SCAN_HEADS_BWD)
        chunks = NCH // SCAN_PARTS_BWD

        if exchange is not None:
            @pl.when((pl.program_id(0) == 0) & (pl.program_id(1) == 0))
            def _():
                exchange.start(*exchange.split(ex_refs))

        @pl.when((pl.program_id(0) == 0) & (pl.program_id(1) == 0))
        def _():
            dwn_ref[...] = jnp.zeros_like(dwn_ref)

        @pl.when(pl.program_id(1) == 0)
        def _():
            carry_ref[...] = jnp.zeros_like(carry_ref)

        wn = wn_ref[...]
        for hh in heads:
            c = _head_cols(hh)
            ov = o_ref[:, c]
            zv = z_ref[:, c]
            g = dy_ref[:, c]
            sig = _sigmoid(zv)
            dz_ref[:, c] = (g * (ov * _rms_scale(ov) * wn) * sig * (1.0 + zv * (1.0 - sig))).astype(BF16)
            do, dwt = _rms_bwd(ov, wn, g * zv * sig)
            do_ref[:, c] = do
            dwn_ref[...] += jnp.sum(dwt, axis=0, keepdims=True)

        def step(t, dstates):
            ci = chunks - 1 - t
            rows = pl.ds(pl.multiple_of(ci * CHUNK, CHUNK), CHUNK)
            cols = [_head_cols(hh) for hh in heads]
            state = [st_ref[hh, ci] for hh in heads]
            dov = [do_ref[rows, cols[hh]] for hh in heads]
            wv = [w_ref[rows, cols[hh]] for hh in heads]
            ws = [_dot(wv[hh], state[hh]) for hh in heads]
            adov = [_dot(a_ref[hh, rows, :], dov[hh], 0, 0) for hh in heads]
            kds = [_dot(kd_ref[rows, cols[hh]], dstates[hh]) for hh in heads]
            dqd = [_dot(dov[hh], state[hh], 1, 1) for hh in heads]
            qdo = [_dot(qd_ref[rows, cols[hh]], dov[hh], 0, 0) for hh in heads]
            vn = [u_ref[rows, cols[hh]] - ws[hh] for hh in heads]
            dvn = [adov[hh] + kds[hh] for hh in heads]
            da = [_dot(dov[hh], vn[hh], 1, 1) for hh in heads]
            dkd = [_dot(vn[hh], dstates[hh], 1, 1) for hh in heads]
            dwv = [_dot(dvn[hh], state[hh], 1, 1) for hh in heads]
            wdv = [_dot(wv[hh], dvn[hh], 0, 0) for hh in heads]
            for hh in heads:
                da_ref[hh, rows, :] = da[hh]
                dqd_ref[rows, cols[hh]] = dqd[hh]
                dkd_ref[rows, cols[hh]] = dkd[hh]
                dgl = jnp.sum(jnp.sum(dstates[hh] * state[hh], axis=1, keepdims=True), axis=0, keepdims=True)
                dgl_ref[hh, ci] = jnp.broadcast_to(dgl, (1, LANES))
                du_ref[rows, cols[hh]] = dvn[hh]
                dw_ref[rows, cols[hh]] = -dwv[hh]
            return tuple(dstates[hh] * gl_ref[hh, ci] + qdo[hh] - wdv[hh] for hh in heads)

        last = lax.fori_loop(0, chunks, step, tuple(carry_ref[hh] for hh in heads))
        for hh in heads:
            carry_ref[hh] = last[hh]

        if exchange is not None:
            @pl.when((pl.program_id(0) == groups - 1) & (pl.program_id(1) == SCAN_PARTS_BWD - 1))
            def _():
                exchange.finish(*exchange.split(ex_refs))

    hs, a_s, gl_s, st_s, gz_s, mix_s = _scan_specs(SCAN_HEADS_BWD, SCAN_PARTS_BWD, True)
    vec = pl.BlockSpec((1, GHD), lambda g, p: (0, 0))
    tok = jax.ShapeDtypeStruct((S, DGDN), F32)
    res = pl.pallas_call(
        body, name="gdn_scan_bwd", grid=(groups, SCAN_PARTS_BWD),
        in_specs=[mix_s, hs, gz_s, vec, hs, hs, hs, hs, a_s, gl_s, st_s, pl.BlockSpec(memory_space=pl.ANY)] + ex_in_specs,
        out_specs=[gz_s, hs, hs, hs, hs, a_s, gl_s, vec] + ex_out_specs,
        out_shape=[jax.ShapeDtypeStruct((S, DPROJ_PAD), BF16), tok, tok, tok, tok,
                   jax.ShapeDtypeStruct((NGH, S, CHUNK), F32), jax.ShapeDtypeStruct((NGH, NCH, 1, LANES), F32),
                   jax.ShapeDtypeStruct((1, GHD), F32)] + ex_out_shape,
        input_output_aliases={11: 0},
        scratch_shapes=[pltpu.VMEM((S // SCAN_PARTS_BWD, SCAN_HEADS_BWD * GHD), F32),
                        pltpu.VMEM((SCAN_HEADS_BWD, GHD, GHD), F32)] + ex_scratch,
        compiler_params=_cparams(),
    )(dmix, o, proj, w_norm, u, w, qd, kd, a, gl, states, dproj, *ex_in)
    return res[:8], res[8:]


def _place():
    return lax.axis_index("x"), lax.axis_index("y"), lax.axis_index("c")


def _other_chips(x, y):
    return [(1 - x, y), (x, 1 - y), (1 - x, 1 - y)]


HBM = pl.BlockSpec(memory_space=pltpu.HBM)
VMEM = pl.BlockSpec(memory_space=pltpu.VMEM)


def _half_rows(ref_or_rows, half):
    rows = ref_or_rows // 2
    return pl.ds(pl.multiple_of(half * rows, rows), rows)


class _Exchange:
    def __init__(self, inputs, out_shape, n_sems, start, finish=None, middle=None, rest=None):
        self.inputs, self.out_shape, self.n_sems, self.start = inputs, out_shape, n_sems, start
        if finish is None:
            def finish(*refs):
                middle(*refs)
                rest(*refs)
        self.finish = finish
        self.middle = middle if middle is not None else (lambda *refs: None)
        self.rest = rest if rest is not None else finish

    def sem_shapes(self):
        return [pltpu.SemaphoreType.DMA((self.n_sems,)), pltpu.SemaphoreType.DMA((self.n_sems,))]

    def split(self, refs):
        n_in, n_out = len(self.inputs), len(self.out_shape)
        return refs[:n_in], refs[n_in:n_in + n_out], refs[n_in + n_out], refs[n_in + n_out + 1]


def _run_exchange(ex, name):
    def body(*refs):
        parts = ex.split(refs)
        ex.start(*parts)
        ex.finish(*parts)

    return pl.pallas_call(
        body, name=name, in_specs=[HBM] * len(ex.inputs), out_specs=[HBM] * len(ex.out_shape), out_shape=ex.out_shape,
        scratch_shapes=ex.sem_shapes(), compiler_params=_cparams(),
    )(*ex.inputs)


def _allgather_exchange(shards, whole=()):
    n, nw = len(shards), len(whole)
    slots = 8

    def plan(src, outs, send_sems, recv_sems):
        x, y, c = _place()
        via_x, via_y, diagonal = _other_chips(x, y)
        id_x, id_y, id_diagonal = [2 * chip[0] + chip[1] for chip in (via_x, via_y, diagonal)]
        me, sibling = (x, y, c), (x, y, 1 - c)

        def rows_of(a, half, quarter):
            total = src[a].shape[0]
            if quarter is None:
                return _half_rows(total, half)
            return pl.ds(pl.multiple_of(half * (total // 2) + quarter * (total // 4), total // 4), total // 4)

        def copy(a, k, chip_index, half, quarter, to, from_src=False):
            rows = rows_of(a, half, quarter)
            dst = outs[a].at[chip_index, rows]
            return pltpu.make_async_remote_copy(
                src_ref=src[a].at[rows] if from_src else dst, dst_ref=dst, send_sem=send_sems.at[slots * a + k],
                recv_sem=recv_sems.at[slots * a + k], device_id=to, device_id_type=MESH)

        def whole_copy(b, k, chip_index, to):
            return pltpu.make_async_remote_copy(
                src_ref=src[n + b], dst_ref=outs[n + b].at[chip_index], send_sem=send_sems.at[slots * n + 3 * b + k],
                recv_sem=recv_sems.at[slots * n + 3 * b + k], device_id=to, device_id_type=MESH)

        first, stages, last = [], [], []
        for a in range(n):
            first += [copy(a, 0, 2 * x + y, c, None, (*via_x, c), True), copy(a, 1, 2 * x + y, c, None, (*via_y, c), True)]
            stages.append([
                (copy(a, 0, id_x, c, None, me),
                 [copy(a, 2, id_x, c, 0, (*via_y, c)), copy(a, 4, id_x, c, None, sibling)]),
                (copy(a, 1, id_y, c, None, me),
                 [copy(a, 3, id_y, c, 1, (*via_x, c)), copy(a, 5, id_y, c, None, sibling)]),
                (copy(a, 2, id_diagonal, c, 0, me), [copy(a, 6, id_diagonal, c, 0, sibling)]),
                (copy(a, 3, id_diagonal, c, 1, me), [copy(a, 7, id_diagonal, c, 1, sibling)]),
            ])
            last += [copy(a, 4, id_x, 1 - c, None, me), copy(a, 5, id_y, 1 - c, None, me),
                     copy(a, 6, id_diagonal, 1 - c, 0, me), copy(a, 7, id_diagonal, 1 - c, 1, me)]
        for b in range(nw):
            for k, (chip, index) in enumerate(((via_x, id_x), (via_y, id_y), (diagonal, id_diagonal))):
                first.append(whole_copy(b, k, 2 * x + y, (*chip, c)))
                last.append(whole_copy(b, k, index, me))
        return first, stages, last

    def start(*refs):
        for cp in plan(*refs)[0]:
            cp.start()

    def pass_on(stages, which):
        for stage in which:
            for per_shard in stages:
                lands, onward = per_shard[stage]
                lands.wait_recv()
                for cp in onward:
                    cp.start()

    def middle(*refs):
        pass_on(plan(*refs)[1], (0, 1))

    def rest(*refs):
        first, stages, last = plan(*refs)
        pass_on(stages, (2, 3))
        for cp in last:
            cp.wait_recv()
        for cp in first + [cp for per_shard in stages for _, onward in per_shard for cp in onward]:
            cp.wait_send()

    out_shape = [jax.ShapeDtypeStruct((NCHIP,) + s.shape, s.dtype) for s in list(shards) + list(whole)]
    return _Exchange(list(shards) + list(whole), out_shape, slots * n + 3 * nw, start, middle=middle, rest=rest)


def _with_own(gathered, own):
    x, y, _ = _place()
    return lax.dynamic_update_index_in_dim(gathered, own, 2 * x + y, axis=0)


def _simple_exchange(inputs, out_shape, copies_of):
    def start(*refs):
        for cp in copies_of(*refs):
            cp.start()

    def finish(*refs):
        for cp in copies_of(*refs):
            cp.wait()

    return _Exchange(list(inputs), out_shape, len(out_shape) * 3, start, finish)


def _pair_exchange(grads):
    def copies_of(src, outs, send_sems, recv_sems):
        x, y, c = _place()
        return [pltpu.make_async_remote_copy(
            src_ref=src[a].at[:, _half_rows(src[a].shape[1], 1 - c)], dst_ref=outs[a], send_sem=send_sems.at[a],
            recv_sem=recv_sems.at[a], device_id=(x, y, 1 - c), device_id_type=MESH) for a in range(len(src))]

    return _simple_exchange(
        grads, [jax.ShapeDtypeStruct((g.shape[0], g.shape[1] // 2, g.shape[2]), g.dtype) for g in grads], copies_of)


def _pair_sum(grads, theirs, name):
    n = len(grads)

    def body(*refs):
        south = lax.axis_index("c") == 0
        for a in range(n):
            g = refs[a][...]
            half = g.shape[0] // 2
            mine = jnp.where(south, g[:half], g[half:])
            refs[2 * n + a][...] = (mine.astype(F32) + refs[n + a][...].astype(F32)).astype(BF16)

    def specs(arrs):
        return [pl.BlockSpec((None,) + g.shape[1:], lambda j: (j, 0, 0)) for g in arrs]

    return pl.pallas_call(
        body, name=name, grid=(NCHIP,), in_specs=specs(grads) + specs(theirs), out_specs=specs(theirs),
        out_shape=[jax.ShapeDtypeStruct(g.shape, BF16) for g in theirs], compiler_params=_cparams(),
    )(*grads, *theirs)


def _chip_exchange(parts):
    def copies_of(src, outs, send_sems, recv_sems):
        x, y, c = _place()
        return [pltpu.make_async_remote_copy(
            src_ref=src[a].at[2 * chip[0] + chip[1]], dst_ref=outs[a].at[k], send_sem=send_sems.at[3 * a + k],
            recv_sem=recv_sems.at[3 * a + k], device_id=(*chip, c), device_id_type=MESH)
            for a in range(len(src)) for k, chip in enumerate(_other_chips(x, y))]

    return _simple_exchange(parts, [jax.ShapeDtypeStruct((NCHIP - 1,) + p.shape[1:], p.dtype) for p in parts], copies_of)


def _chip_sum(parts, received, exchange=None):
    n = len(parts)
    steps = 4
    ex_in, ex_in_specs, ex_out_specs, ex_out_shape, ex_scratch = _hosted(exchange)

    def body(*refs):
        ex_refs = refs[2 * n:2 * n + len(ex_in)] + refs[3 * n + len(ex_in):]
        if exchange is not None:
            @pl.when(pl.program_id(0) == 0)
            def _():
                exchange.start(*exchange.split(ex_refs))

        chip = 2 * lax.axis_index("x") + lax.axis_index("y")
        for a in range(n):
            p, r = refs[a], refs[n + a]
            own = jnp.where(chip == 0, p[0], jnp.where(chip == 1, p[1], jnp.where(chip == 2, p[2], p[3])))
            refs[2 * n + len(ex_in) + a][...] = ((own.astype(F32) + r[0].astype(F32)) + r[1].astype(F32)) + r[2].astype(F32)

        if exchange is not None:
            @pl.when(pl.program_id(0) == steps - 1)
            def _():
                exchange.finish(*exchange.split(ex_refs))

    def specs(arrs):
        return [pl.BlockSpec((g.shape[0], g.shape[1] // steps, g.shape[2]), lambda i: (0, i, 0)) for g in arrs]

    out_specs = [pl.BlockSpec((g.shape[1] // steps, g.shape[2]), lambda i: (i, 0)) for g in parts]
    res = pl.pallas_call(
        body, name="grads_chip_sum", grid=(steps,), in_specs=specs(parts) + specs(received) + ex_in_specs,
        out_specs=out_specs + ex_out_specs, out_shape=[jax.ShapeDtypeStruct(g.shape[1:], F32) for g in parts] + ex_out_shape,
        scratch_shapes=ex_scratch, compiler_params=_cparams(),
    )(*parts, *received, *ex_in)
    return res[:n], res[n:]


def _pair_share(halves):
    def copies_of(src, outs, send_sems, recv_sems):
        x, y, c = _place()
        return [pltpu.make_async_remote_copy(
            src_ref=src[a], dst_ref=outs[a], send_sem=send_sems.at[a], recv_sem=recv_sems.at[a],
            device_id=(x, y, 1 - c), device_id_type=MESH) for a in range(len(src))]

    return _simple_exchange(halves, [jax.ShapeDtypeStruct(h.shape, F32) for h in halves], copies_of)


def _adamw_math(w, g, m, v):
    nm = ADAM_B1 * m + (1.0 - ADAM_B1) * g
    nv = ADAM_B2 * v + (1.0 - ADAM_B2) * jnp.square(g)
    m_hat = nm / (1.0 - ADAM_B1 ** ADAM_STEP)
    v_hat = nv / (1.0 - ADAM_B2 ** ADAM_STEP)
    return -ADAM_LR * (m_hat / (jnp.sqrt(v_hat) + ADAM_EPS) + ADAM_WD * w), nm, nv


def _adamw_big(ws, g_mine, g_theirs, ms, vs, exchange=None):
    n = len(ws)
    steps = 8
    ex_in, ex_in_specs, ex_out_specs, ex_out_shape, ex_scratch = _hosted(exchange)

    def body(*refs):
        ex_refs = refs[5 * n:5 * n + len(ex_in)] + refs[9 * n + len(ex_in):]
        outs = refs[5 * n + len(ex_in):9 * n + len(ex_in)]
        if exchange is not None:
            @pl.when(pl.program_id(0) == 0)
            def _():
                exchange.start(*exchange.split(ex_refs))

        own_half = (pl.program_id(0) // (steps // 2)) == lax.axis_index("c")
        for a in range(n):
            g = jnp.where(own_half, refs[n + a][...], refs[2 * n + a][...])
            d, nm, nv = _adamw_math(refs[a][...], g, refs[3 * n + a][...], refs[4 * n + a][...])
            outs[a][...] = g
            outs[n + a][...] = d
            outs[2 * n + a][...] = nm
            outs[3 * n + a][...] = nv

        if exchange is not None:
            @pl.when(pl.program_id(0) == steps - 1)
            def _():
                exchange.finish(*exchange.split(ex_refs))

    specs = [pl.BlockSpec((w.shape[0] // steps, w.shape[1]), lambda i: (i, 0)) for w in ws]
    half_specs = [pl.BlockSpec((g.shape[0] // (steps // 2), g.shape[1]), lambda i: (i % (steps // 2), 0)) for g in g_mine]
    shapes = [jax.ShapeDtypeStruct(w.shape, F32) for w in ws]
    res = pl.pallas_call(
        body, name="adamw_big", grid=(steps,), in_specs=specs + half_specs * 2 + specs * 2 + ex_in_specs,
        out_specs=specs * 4 + ex_out_specs, out_shape=shapes * 4 + ex_out_shape, scratch_shapes=ex_scratch,
        compiler_params=_cparams(),
    )(*ws, *g_mine, *g_theirs, *ms, *vs, *ex_in)
    return res[:n], res[n:2 * n], res[2 * n:3 * n], res[3 * n:4 * n], res[4 * n:]


def _adamw_in(w, g_mine, g_theirs, m, v):
    half = D // 2

    def body(w_ref, gm_ref, gt_ref, m_ref, v_ref, g_out, d_out, nm_out, nv_out, g_ref):
        south = lax.axis_index("c") == 0
        g_ref[0:half, :] = jnp.where(south, gm_ref[...], gt_ref[...])
        g_ref[half:D, :] = jnp.where(south, gt_ref[...], gm_ref[...])
        g = g_ref[0:CW, :]
        d, nm, nv = _adamw_math(w_ref[...], g, m_ref[...], v_ref[...])
        g_out[...] = g
        d_out[...] = d
        nm_out[...] = nm
        nv_out[...] = nv

    spec = pl.BlockSpec((CW, LANES), lambda i: (0, i))
    half_spec = pl.BlockSpec((half, LANES), lambda i: (0, i))
    return pl.pallas_call(
        body, name="adamw_in", grid=(D // LANES,), in_specs=[spec, half_spec, half_spec, spec, spec], out_specs=[spec] * 4,
        out_shape=[jax.ShapeDtypeStruct((CW, D), F32)] * 4, scratch_shapes=[pltpu.VMEM((D, LANES), F32)],
        compiler_params=_cparams(),
    )(w, g_mine, g_theirs, m, v)


NORM_NAMES = ("pre_mix_norm", "post_mix_norm", "pre_mlp_norm", "post_mlp_norm")
SMALL_NAMES = NORM_NAMES + ("gdn_conv_w", "fox_f_bias", "gdn_dt_bias", "gdn_a_log", "fox_out_norm", "gdn_out_norm")
CONV_COLS = 3 * DGDN // NCHIP


def _small_gather(d_norms, d_conv, sums, d_fox_norm, d_gdn_norm, loss_row):
    n_arrays = 6
    n_remote = n_arrays * (NDEV - 1)

    def copies_of(src, outs, send_sems, recv_sems):
        x, y, c = _place()
        me = 4 * x + 2 * y + c

        def from_me(chip_index):
            cols = pl.ds(pl.multiple_of(chip_index * CONV_COLS, LANES), CONV_COLS)
            return [src[0], src[1].at[:, cols], src[2], src[3], src[4], src[5]]

        local = [pltpu.make_async_copy(s, outs[a].at[me], send_sems.at[n_remote + a]) for a, s in enumerate(from_me(2 * x + y))]
        remote = []
        for k in range(1, NDEV):
            px, py, pc = x ^ ((k >> 2) & 1), y ^ ((k >> 1) & 1), c ^ (k & 1)
            remote += [pltpu.make_async_remote_copy(
                src_ref=s, dst_ref=outs[a].at[me], send_sem=send_sems.at[n_arrays * (k - 1) + a],
                recv_sem=recv_sems.at[n_arrays * (k - 1) + a], device_id=(px, py, pc), device_id_type=MESH)
                for a, s in enumerate(from_me(2 * px + py))]
        return local + remote

    def start(*refs):
        for cp in copies_of(*refs):
            cp.start()

    def finish(*refs):
        for cp in copies_of(*refs):
            cp.wait()

    shapes = [(4, D), (CONV_K, CONV_COLS), (8, LANES), (1, LANES), (1, LANES), (1, LANES)]
    return _Exchange([d_norms, d_conv, sums, d_fox_norm, d_gdn_norm, loss_row],
                     [jax.ShapeDtypeStruct((NDEV,) + s, F32) for s in shapes], n_remote + n_arrays, start, finish)


def _small_adamw(gathered, ws, ms, vs):
    n = len(SMALL_NAMES)
    ng = len(gathered)

    def body(*refs):
        def total(buf):
            acc = buf[0]
            for i in range(1, NDEV):
                acc = acc + buf[i]
            return acc

        t_norms, t_conv, t_sums, t_fn, t_gn, t_loss = [total(r) for r in refs[:ng]]
        w_refs, m_refs, v_refs = refs[ng:ng + n], refs[ng + n:ng + 2 * n], refs[ng + 2 * n:ng + 3 * n]
        outs = refs[ng + 3 * n:]
        outs[4 * n][...] = t_loss
        grads = [t_norms[i:i + 1, :] for i in range(4)] + [
            t_conv, t_sums[0:1, 0:NFH], t_sums[1:2, 0:NGH], t_sums[2:3, 0:NGH], t_fn[:, 0:FHD], t_gn]
        for a in range(n):
            d, nm, nv = _adamw_math(w_refs[a][...], grads[a], m_refs[a][...], v_refs[a][...])
            outs[a][...] = grads[a]
            outs[n + a][...] = d
            outs[2 * n + a][...] = nm
            outs[3 * n + a][...] = nv

    def whole(arr):
        return pl.BlockSpec(arr.shape, lambda i: (0,) * arr.ndim)

    res = pl.pallas_call(
        body, name="small_adamw", grid=(1,), in_specs=[whole(t) for t in gathered] + [whole(w) for w in ws] * 3,
        out_specs=[whole(w) for w in ws] * 4 + [pl.BlockSpec((1, LANES), lambda i: (0, 0))],
        out_shape=[jax.ShapeDtypeStruct(w.shape, F32) for w in ws] * 4 + [jax.ShapeDtypeStruct((1, LANES), F32)],
        compiler_params=_cparams(),
    )(*gathered, *ws, *ms, *vs)
    return res[:n], res[n:2 * n], res[2 * n:3 * n], res[3 * n:4 * n], res[4 * n]


CW = DPROJ // NCHIP
PROJ_RUNS = tuple((part * DFOX + hp * LANES, part * DFOX + (hp + 1) * LANES, (3 * hp + part) * LANES)
                  for hp in range(NPAIR) for part in range(3)) + (
    (1536, 1544, BLK_SMALL * LANES), (1544, 3080, BLK_GDN * LANES), (3080, 3088, BLK_SMALL * LANES + 8),
    (3088, 3600, BLK_GZ * LANES))


def _proj_pieces():
    pieces = []
    for lo, hi, at in PROJ_RUNS:
        while lo < hi:
            j = lo // CW
            end = min(hi, (j + 1) * CW)
            pieces.append((j, lo - j * CW, at, end - lo))
            at, lo = at + end - lo, end
    return pieces


RT = 256


def _to_padded_rows(gathered):
    def body(src_ref, out_ref, blocks_ref, rows_ref):
        blocks_ref[...] = src_ref[...].astype(F32)
        rows_ref[...] = jnp.zeros_like(rows_ref)
        for j, start, at, n in _proj_pieces():
            rows_ref[at:at + n, :] = blocks_ref[j, start:start + n, :]
        out_ref[...] = rows_ref[...].astype(out_ref.dtype)

    return pl.pallas_call(
        body, name="proj_rows_in", grid=(D // RT,), in_specs=[pl.BlockSpec((NCHIP, D, RT), lambda i: (0, 0, i))],
        out_specs=pl.BlockSpec((DPROJ_PAD, RT), lambda i: (0, i)), out_shape=jax.ShapeDtypeStruct((DPROJ_PAD, D), gathered.dtype),
        scratch_shapes=[pltpu.VMEM((NCHIP, D, RT), F32), pltpu.VMEM((DPROJ_PAD, RT), F32)], compiler_params=_cparams(),
    )(gathered)


def _from_padded_rows(w):
    def body(src_ref, out_ref, rows_ref, blocks_ref):
        rows_ref[...] = src_ref[...].astype(F32)
        blocks_ref[...] = jnp.zeros_like(blocks_ref)
        for j, start, at, n in _proj_pieces():
            blocks_ref[j, start:start + n, :] = rows_ref[at:at + n, :]
        out_ref[...] = blocks_ref[...].astype(out_ref.dtype)

    return pl.pallas_call(
        body, name="proj_rows_out", grid=(D // RT,), in_specs=[pl.BlockSpec((DPROJ_PAD, RT), lambda i: (0, i))],
        out_specs=pl.BlockSpec((NCHIP, D, RT), lambda i: (0, 0, i)), out_shape=jax.ShapeDtypeStruct((NCHIP, D, D), w.dtype),
        scratch_shapes=[pltpu.VMEM((DPROJ_PAD, RT), F32), pltpu.VMEM((NCHIP, D, RT), F32)], compiler_params=_cparams(),
    )(w)


def _local_step(x, target, first_weights, late_weights, reduce_late, reduce_in, pre_mix_norm, fox_f_bias, fox_out_norm,
                gdn_a_log, gdn_dt_bias, gdn_out_norm, post_mix_norm, pre_mlp_norm, post_mlp_norm):
    bias_vec = jnp.zeros((1, LANES), F32).at[0, 0:NFH].set(fox_f_bias).at[0, LANE_G:LANE_G + NGH].set(gdn_dt_bias)
    alog_vec = jnp.zeros((1, LANES), F32).at[0, LANE_G:LANE_G + NGH].set(gdn_a_log)
    w2 = jnp.concatenate([fox_out_norm, fox_out_norm], axis=1)

    h, first = _pre_norm(x, pre_mix_norm, exchange=first_weights[0])
    win_p, conv_w = first_weights[1](first)
    proj = _matmul(h, win_p, tb=True, tm=2048, tn=768, tk=1024, name="mm_proj", exchange=late_weights[0])
    proj, late_a = proj if late_weights[0] is not None else (proj, [])
    gates = _gates(proj, bias_vec, alog_vec)
    mix, fox_o, lse, late_b = _fox_fwd(proj, gates, w2, exchange=late_weights[1])
    qkv = _gdn_pre(proj, conv_w)
    (u, w, qd, kd, a_intra, gl, t_inv), late_c = _gdn_prep(qkv, gates, exchange=late_weights[2])
    wout, wup3, wdown = late_weights[3](late_a, late_b, late_c)
    mix, gdn_raw, states = _gdn_scan(u, w, qd, kd, a_intra, gl, proj, gdn_out_norm, mix)
    mixed = _matmul(mix, wout, tm=2048, tk=1024, name="mm_out")
    x1, h2 = _post_mix(x, mixed, post_mix_norm, pre_mlp_norm)

    def relu2(acc):
        r = jnp.maximum(acc, 0.0)
        return r, r * r

    up_relu, act = _matmul(h2, wup3, b3=True, tm=1024, tn=1024, tk=1024, out_dtypes=(BF16, BF16), epilogue=relu2,
                           name="mm_up")
    y = _matmul(act, wdown, tm=1024, tk=DFF, name="mm_down")
    dx2, dy, d_post_mlp, loss_row = _loss_head(x1, y, post_mlp_norm, target)

    dwdown = _matmul(act, dy, ta=True, tm=1024, tn=1024, tk=2048, out_dtypes=(BF16,), name="mm_dwdown")

    def relu2_bwd(acc, r):
        return (acc * 2.0 * r.astype(F32),)

    dup = _matmul(dy, wdown, tb=True, tm=1024, tn=1024, tk=1024, out_dtypes=(BF16,), extra=(up_relu,), epilogue=relu2_bwd,
                  name="mm_dact")
    dwup3 = _matmul(h2, dup, ta=True, tm=1024, tn=1024, tk=2048, out_dtypes=(BF16,), o3=True, name="mm_dwup")
    dh2 = _matmul(dup, wup3, tb=True, b3=True, tm=1024, tk=DFF, name="mm_dh2")
    dx1, dmixed, d_pre_mlp, d_post_mix = _mid_bwd(dh2, x1, pre_mlp_norm, dx2, mixed, post_mix_norm)
    dwout = _matmul(mix, dmixed, ta=True, tm=1024, tn=1024, tk=2048, out_dtypes=(BF16,), name="mm_dwout")
    dmix = _matmul(dmixed, wout, tb=True, tm=2048, tk=1024, name="mm_dmix")

    dfox, delta, d_fox_norm, from_sibling = _fox_norm_bwd(fox_o, dmix, w2, exchange=reduce_late[0](dwout, dwup3, dwdown))
    dproj, dcum_fox, reduced_a = _fox_bwd(proj, dfox, gates, lse, delta, exchange=reduce_late[1](from_sibling))
    (dproj, du, dw, dqd, dkd, da, dgl, d_gdn_norm), reduced_b = _gdn_scan_bwd(
        dmix, gdn_raw, proj, gdn_out_norm, u, w, qd, kd, a_intra, gl, states, dproj, exchange=reduce_late[2]())
    dqkv, dgates_gdn, reduced_c = _gdn_prep_bwd(qkv, gates, t_inv, du, dw, dqd, dkd, da, dgl, exchange=reduce_late[3]())
    reduced_late = (reduced_a, reduced_b, reduced_c)
    dproj, d_conv = _gdn_pre_bwd(proj, conv_w, dqkv, dproj)
    dproj, sums = _gates_bwd(proj, bias_vec, alog_vec, dgates_gdn, dcum_fox, dproj)

    dwin_p = _matmul(dproj, h, ta=True, tm=1280, tn=1024, tk=2048, out_dtypes=(BF16,), name="mm_dwin")
    exchange_in = reduce_in(dwin_p)
    dh = _matmul(dproj, win_p, tm=1024, tk=DPROJ_PAD, name="mm_dh", exchange=exchange_in)
    dh, reduced_in = dh if exchange_in is not None else (dh, [])
    grad_x, d_pre_mix = _pre_norm_bwd(dh, x, pre_mix_norm, dx1)

    d_norms = jnp.concatenate([d_pre_mix, d_post_mix, d_pre_mlp, d_post_mlp], axis=0)
    return grad_x, (d_norms, d_conv, sums, d_fox_norm, d_gdn_norm, loss_row), reduced_late, reduced_in


def kernel(x, pre_mix_norm, w_in, fox_f_bias, fox_out_norm, gdn_conv_w, gdn_a_log, gdn_dt_bias, gdn_out_norm, w_out, post_mix_norm, pre_mlp_norm, w_up, w_down, post_mlp_norm, loss_target, m_pre_mix_norm, m_w_in, m_fox_f_bias, m_fox_out_norm, m_gdn_conv_w, m_gdn_a_log, m_gdn_dt_bias, m_gdn_out_norm, m_w_out, m_post_mix_norm, m_pre_mlp_norm, m_w_up, m_w_down, m_post_mlp_norm, v_pre_mix_norm, v_w_in, v_fox_f_bias, v_fox_out_norm, v_gdn_conv_w, v_gdn_a_log, v_gdn_dt_bias, v_gdn_out_norm, v_w_out, v_post_mix_norm, v_pre_mlp_norm, v_w_up, v_w_down, v_post_mlp_norm):
    weights = dict(pre_mix_norm=pre_mix_norm, w_in=w_in, fox_f_bias=fox_f_bias, fox_out_norm=fox_out_norm, gdn_conv_w=gdn_conv_w,
                   gdn_a_log=gdn_a_log, gdn_dt_bias=gdn_dt_bias, gdn_out_norm=gdn_out_norm, w_out=w_out, post_mix_norm=post_mix_norm,
                   pre_mlp_norm=pre_mlp_norm, w_up=w_up, w_down=w_down, post_mlp_norm=post_mlp_norm)
    m_in = dict(pre_mix_norm=m_pre_mix_norm, w_in=m_w_in, fox_f_bias=m_fox_f_bias, fox_out_norm=m_fox_out_norm, gdn_conv_w=m_gdn_conv_w,
                gdn_a_log=m_gdn_a_log, gdn_dt_bias=m_gdn_dt_bias, gdn_out_norm=m_gdn_out_norm, w_out=m_w_out, post_mix_norm=m_post_mix_norm,
                pre_mlp_norm=m_pre_mlp_norm, w_up=m_w_up, w_down=m_w_down, post_mlp_norm=m_post_mlp_norm)
    v_in = dict(pre_mix_norm=v_pre_mix_norm, w_in=v_w_in, fox_f_bias=v_fox_f_bias, fox_out_norm=v_fox_out_norm, gdn_conv_w=v_gdn_conv_w,
                gdn_a_log=v_gdn_a_log, gdn_dt_bias=v_gdn_dt_bias, gdn_out_norm=v_gdn_out_norm, w_out=v_w_out, post_mix_norm=v_post_mix_norm,
                pre_mlp_norm=v_pre_mlp_norm, w_up=v_w_up, w_down=v_w_down, post_mlp_norm=v_post_mlp_norm)
    order_w = ("pre_mix_norm", "w_in", "fox_f_bias", "fox_out_norm", "gdn_conv_w", "gdn_a_log", "gdn_dt_bias", "gdn_out_norm", "w_out",
               "post_mix_norm", "pre_mlp_norm", "w_up", "w_down", "post_mlp_norm")
    big = ("w_in", "w_out", "w_up", "w_down")

    def row(v):
        return v if v.ndim == 2 else v.reshape(1, -1)

    win_shard = jnp.pad(w_in.T.astype(BF16), ((0, D - CW), (0, 0)))

    def resolve_first(gathered):
        win_g, conv_g = gathered
        return (_to_padded_rows(_with_own(win_g, win_shard)),
                _with_own(conv_g, gdn_conv_w).transpose(1, 0, 2).reshape(CONV_K, 3 * DGDN))

    late_shards = [weights[n].astype(BF16) for n in big[1:]]

    def resolve_late(*gathered):
        wout_g, wup3, wdown_g = [_with_own(g[0], own) for g, own in zip(gathered, late_shards)]
        return wout_g.reshape(D, D), wup3, wdown_g.reshape(DFF, D)

    pair_sums, late_blocks = {}, []

    def pair_summed(names, blocks, theirs):
        for n, s in zip(names, _pair_sum(blocks, theirs, "grads_pair_sum_" + names[0])):
            pair_sums[n] = s

    def late_pair_exchange(dwout, dwup3, dwdown):
        late_blocks.extend([dwout.reshape(NCHIP, D // NCHIP, D), dwup3, dwdown.reshape(NCHIP, DFF // NCHIP, D)])
        return _pair_exchange(late_blocks)

    def late_chip_exchange(theirs):
        pair_summed(big[1:], late_blocks, theirs)
        return _chip_exchange([pair_sums["w_up"], pair_sums["w_down"]])

    def reduce_in(dwin_p):
        blocks = [_from_padded_rows(dwin_p)]
        pair_summed(big[:1], blocks, _run_exchange(_pair_exchange(blocks), "grads_pair_exchange_w_in"))
        return _chip_exchange([pair_sums["w_in"]])

    grad_x, small, received_late, received_in = _local_step(
        x[0], loss_target[0], (_allgather_exchange([win_shard], whole=[gdn_conv_w]), resolve_first),
        tuple(_allgather_exchange([shard]) for shard in late_shards) + (resolve_late,),
        (late_pair_exchange, late_chip_exchange, lambda: None, lambda: _chip_exchange([pair_sums["w_out"]])),
        reduce_in, row(pre_mix_norm), fox_f_bias, row(fox_out_norm), gdn_a_log, gdn_dt_bias,
        row(gdn_out_norm), row(post_mix_norm), row(pre_mlp_norm), row(post_mlp_norm))
    received_mlp, _, received_out = received_late

    g_mine, small_gathered = _chip_sum(
        [pair_sums[n] for n in big], list(received_in[:1]) + list(received_out[:1]) + list(received_mlp[:2]),
        exchange=_small_gather(*small))
    g_theirs = _run_exchange(_pair_share(g_mine), "grads_pair_share")

    g_big, d_big, nm_big, nv_big, _ = _adamw_big(
        [weights[n] for n in big[1:]], g_mine[1:], g_theirs[1:], [m_in[n] for n in big[1:]], [v_in[n] for n in big[1:]])
    in_t = _adamw_in(w_in.T, g_mine[0], g_theirs[0], m_w_in.T, v_w_in.T)
    g_small, d_small, nm_small, nv_small, loss_total = _small_adamw(
        small_gathered, [row(weights[n]) for n in SMALL_NAMES], [row(m_in[n]) for n in SMALL_NAMES],
        [row(v_in[n]) for n in SMALL_NAMES])

    grads, delta, new_m, new_v = {}, {}, {}, {}
    grads["w_in"], delta["w_in"], new_m["w_in"], new_v["w_in"] = [t.T for t in in_t]
    for i, n in enumerate(big[1:]):
        grads[n], delta[n], new_m[n], new_v[n] = g_big[i], d_big[i], nm_big[i], nv_big[i]
    for i, n in enumerate(SMALL_NAMES):
        shape = weights[n].shape
        grads[n], delta[n], new_m[n], new_v[n] = (g_small[i].reshape(shape), d_small[i].reshape(shape),
                                                  nm_small[i].reshape(shape), nv_small[i].reshape(shape))
    return (loss_total[0, 0], grad_x[None], *[grads[n] for n in order_w], *[delta[n] for n in order_w], *[new_m[n] for n in order_w],
            *[new_v[n] for n in order_w])
```

```python
import jax
import jax.numpy as jnp
from jax import lax
from jax.experimental import pallas as pl
from jax.experimental.pallas import tpu as pltpu

F32 = jnp.float32
BF16 = jnp.bfloat16
MESH = pl.DeviceIdType.MESH

S = 2048
D = 1024
NFH, FHD = 8, 64
NPAIR = NFH // 2
NGH, GHD = 4, 128
DFOX = NFH * FHD
DGDN = NGH * GHD
CHUNK = 64
NCH = S // CHUNK
CONV_K = 4
DFF = 4 * D
EPS = 1e-6
DPROJ = 3600
LANES = 128
DPROJ_PAD = 3840
BLK_GDN = 12
BLK_GZ = 24
BLK_SMALL = 28
NCHIP = 4
NDEV = 8
VMEM_LIMIT = 56 * 1024 * 1024

ADAM_LR = 0.001
ADAM_B1 = 0.9
ADAM_B2 = 0.999
ADAM_EPS = 1e-08
ADAM_WD = 0.01
ADAM_STEP = 10


def _cparams(**kw):
    return pltpu.CompilerParams(vmem_limit_bytes=VMEM_LIMIT, **kw)


def _dn(ca, cb):
    return (((ca,), (cb,)), ((), ()))


def _dot(a, b, ca=1, cb=0):
    return lax.dot_general(a.astype(BF16), b.astype(BF16), _dn(ca, cb), preferred_element_type=F32)


def _hdot(a, b, ca=1, cb=0):
    return lax.dot_general(a.astype(F32), b.astype(F32), _dn(ca, cb), precision=lax.Precision.HIGHEST,
                           preferred_element_type=F32)


def _dot3(a, b, ca=1, cb=0):
    a_hi, b_hi = a.astype(BF16), b.astype(BF16)
    a_lo, b_lo = (a - a_hi.astype(F32)).astype(BF16), (b - b_hi.astype(F32)).astype(BF16)
    dn = _dn(ca, cb)
    return (lax.dot_general(a_hi, b_hi, dn, preferred_element_type=F32)
            + (lax.dot_general(a_hi, b_lo, dn, preferred_element_type=F32)
               + lax.dot_general(a_lo, b_hi, dn, preferred_element_type=F32)))


@jax.custom_vjp
def _mm_nn(a, b):
    return _dot(a, b, 1, 0)


def _mm_nn_fwd(a, b):
    return _dot(a, b, 1, 0), (a, b)


def _mm_nn_bwd(res, g):
    a, b = res
    return _dot(g, b, 1, 1), _dot(a, g, 0, 0)


_mm_nn.defvjp(_mm_nn_fwd, _mm_nn_bwd)


@jax.custom_vjp
def _mm_nt(a, b):
    return _dot(a, b, 1, 1)


def _mm_nt_fwd(a, b):
    return _dot(a, b, 1, 1), (a, b)


def _mm_nt_bwd(res, g):
    a, b = res
    return _dot(g, b, 1, 0), _dot(g, a, 0, 0)


_mm_nt.defvjp(_mm_nt_fwd, _mm_nt_bwd)


@jax.custom_vjp
def _saved_inverse(m, t_inv):
    del m
    return t_inv


def _saved_inverse_fwd(m, t_inv):
    del m
    return t_inv, t_inv


def _saved_inverse_bwd(t_inv, g):
    return -_dot3(_dot3(t_inv, g, 0, 0), t_inv, 1, 1), jnp.zeros_like(t_inv)


_saved_inverse.defvjp(_saved_inverse_fwd, _saved_inverse_bwd)


def _sigmoid(z):
    return 1.0 / (1.0 + jnp.exp(-z))


def _softplus(z):
    return jnp.maximum(z, 0.0) + jnp.log(1.0 + jnp.exp(-jnp.abs(z)))


def _silu(z):
    return z * _sigmoid(z)


def _rms_scale(x):
    return lax.rsqrt(jnp.mean(x * x, axis=-1, keepdims=True) + EPS)


def _rms_bwd(x, w, g):
    r = _rms_scale(x)
    gw = g * w
    dx = r * gw - x * (r * r * r) * jnp.mean(gw * x, axis=-1, keepdims=True)
    return dx, g * x * r


def _matmul(a, b, *, name, ta=False, tb=False, tm=512, tn=512, tk=512, out_dtypes=(F32,), b3=False, o3=False,
            extra=(), epilogue=None, exchange=None):
    m, k = (a.shape[1], a.shape[0]) if ta else a.shape
    if b3:
        n = b.shape[1] if tb else b.shape[0] * b.shape[2]
        kb = b.shape[0] * b.shape[2] if tb else b.shape[1]
    else:
        n, kb = (b.shape[0], b.shape[1]) if tb else (b.shape[1], b.shape[0])
    assert kb == k, (name, kb, k)
    tm, tn, tk = min(tm, m), min(tn, n), min(tk, k)
    assert m % tm == 0 and n % tn == 0 and k % tk == 0, (name, m, n, k, tm, tn, tk)
    nk = k // tk
    whole_k_blocks = b3 and tb and not ta and nk == 1 and b.shape[0] > 1
    n_extra = len(extra)
    n_out = len(out_dtypes)
    grid = (m // tm, n // tn, nk)
    ex_in, ex_in_specs, ex_out_specs, ex_out_shape, ex_scratch = _hosted(exchange)

    def body(*refs):
        a_ref, b_ref = refs[0], refs[1]
        extra_refs = refs[2:2 + n_extra]
        first_out = 2 + n_extra + len(ex_in)
        out_refs = refs[first_out:first_out + n_out]
        ex_refs = refs[2 + n_extra:first_out] + refs[first_out + n_out:first_out + n_out + len(ex_out_shape)] + refs[-2:]
        step = [pl.program_id(d) for d in range(3)]

        if exchange is not None:
            @pl.when((step[0] == 0) & (step[1] == 0) & (step[2] == 0))
            def _():
                exchange.start(*exchange.split(ex_refs))

        def finish(acc):
            outs = (acc,) if epilogue is None else epilogue(acc, *[r[...] for r in extra_refs])
            for o_ref, val in zip(out_refs, outs):
                o_ref[...] = val.astype(o_ref.dtype)

        if whole_k_blocks:
            width = b.shape[2]
            part = _dot(a_ref[:, 0:width], b_ref[0], 1, 1)
            for blk in range(1, b.shape[0]):
                part = part + _dot(a_ref[:, blk * width:(blk + 1) * width], b_ref[blk], 1, 1)
        else:
            part = _dot(a_ref[...], b_ref[...], 0 if ta else 1, 1 if tb else 0)
        if nk == 1:
            finish(part)
        else:
            acc_ref = refs[first_out + n_out + len(ex_out_shape)]

            @pl.when(step[2] == 0)
            def _():
                acc_ref[...] = part

            @pl.when(step[2] > 0)
            def _():
                acc_ref[...] += part

            @pl.when(step[2] == nk - 1)
            def _():
                finish(acc_ref[...])

        if exchange is not None:
            flat = (step[0] * grid[1] + step[1]) * nk + step[2]
            total = grid[0] * grid[1] * nk

            @pl.when(flat == total // 2)
            def _():
                exchange.middle(*exchange.split(ex_refs))

            @pl.when(flat == total - 1)
            def _():
                exchange.rest(*exchange.split(ex_refs))

    a_spec = pl.BlockSpec((tk, tm), lambda i, j, kk: (kk, i)) if ta else pl.BlockSpec((tm, tk), lambda i, j, kk: (i, kk))
    if whole_k_blocks:
        b_spec = pl.BlockSpec((b.shape[0], tn, b.shape[2]), lambda i, j, kk: (0, j, 0))
    elif b3 and tb:
        assert b.shape[2] == tk
        b_spec = pl.BlockSpec((None, tn, tk), lambda i, j, kk: (kk, j, 0))
    elif b3:
        assert b.shape[2] == tn
        b_spec = pl.BlockSpec((None, tk, tn), lambda i, j, kk: (j, kk, 0))
    elif tb:
        b_spec = pl.BlockSpec((tn, tk), lambda i, j, kk: (j, kk))
    else:
        b_spec = pl.BlockSpec((tk, tn), lambda i, j, kk: (kk, j))
    tile = pl.BlockSpec((tm, tn), lambda i, j, kk: (i, j))
    out_specs = [tile] * n_out
    out_shape = [jax.ShapeDtypeStruct((m, n), dt) for dt in out_dtypes]
    if o3:
        out_specs[0] = pl.BlockSpec((None, tm, tn), lambda i, j, kk: (j, i, 0))
        out_shape[0] = jax.ShapeDtypeStruct((n // tn, m, tn), out_dtypes[0])
    res = pl.pallas_call(
        body, name=name, grid=grid,
        in_specs=[a_spec, b_spec] + [tile] * n_extra + ex_in_specs, out_specs=out_specs + ex_out_specs,
        out_shape=out_shape + ex_out_shape,
        scratch_shapes=([pltpu.VMEM((tm, tn), F32)] if nk > 1 else []) + ex_scratch,
        compiler_params=_cparams(),
    )(a, b, *extra, *ex_in)
    if exchange is not None:
        return (res[0] if n_out == 1 else res[:n_out]), res[n_out:]
    return res[0] if n_out == 1 else res


TR = 256


def _row_spec(cols):
    return pl.BlockSpec((TR, cols), lambda i: (i, 0))


def _vec_spec(cols):
    return pl.BlockSpec((1, cols), lambda i: (0, 0))


def _pre_norm(x, w, exchange=None):
    ex_in, ex_in_specs, ex_out_specs, ex_out_shape, ex_scratch = _hosted(exchange)

    def body(*refs):
        x_ref, w_ref, h_ref = refs[0], refs[1], refs[2 + len(ex_in)]
        ex_refs = refs[2:2 + len(ex_in)] + refs[3 + len(ex_in):]
        if exchange is not None:
            @pl.when(pl.program_id(0) == 0)
            def _():
                exchange.start(*exchange.split(ex_refs))

        xv = x_ref[...]
        h_ref[...] = (xv * _rms_scale(xv) * w_ref[...]).astype(BF16)

        if exchange is not None:
            @pl.when(pl.program_id(0) == S // TR - 1)
            def _():
                exchange.finish(*exchange.split(ex_refs))

    res = pl.pallas_call(
        body, name="pre_norm", grid=(S // TR,), in_specs=[_row_spec(D), _vec_spec(D)] + ex_in_specs,
        out_specs=[_row_spec(D)] + ex_out_specs, out_shape=[jax.ShapeDtypeStruct((S, D), BF16)] + ex_out_shape,
        scratch_shapes=ex_scratch, compiler_params=_cparams(),
    )(x, w, *ex_in)
    return res[0], res[1:]


def _post_mix(x, mixed, w_post, w_pre_mlp):
    def body(x_ref, m_ref, wp_ref, wm_ref, x1_ref, h2_ref):
        mv = m_ref[...]
        x1 = x_ref[...] + mv * _rms_scale(mv) * wp_ref[...]
        x1_ref[...] = x1
        h2_ref[...] = (x1 * _rms_scale(x1) * wm_ref[...]).astype(BF16)

    return pl.pallas_call(
        body, name="post_mix", grid=(S // TR,),
        in_specs=[_row_spec(D), _row_spec(D), _vec_spec(D), _vec_spec(D)], out_specs=[_row_spec(D), _row_spec(D)],
        out_shape=[jax.ShapeDtypeStruct((S, D), F32), jax.ShapeDtypeStruct((S, D), BF16)], compiler_params=_cparams(),
    )(x, mixed, w_post, w_pre_mlp)


def _loss_head(x1, y, w_post_mlp, target):
    def body(x1_ref, y_ref, w_ref, t_ref, dx2_ref, dy_ref, dw_ref, loss_ref):
        i = pl.program_id(0)
        yv = y_ref[...]
        w = w_ref[...]
        x2 = x1_ref[...] + yv * _rms_scale(yv) * w
        err = x2 - t_ref[...]
        dx2 = err * (1.0 / D)
        dx2_ref[...] = dx2
        dy, dwt = _rms_bwd(yv, w, dx2)
        dy_ref[...] = dy.astype(BF16)

        @pl.when(i == 0)
        def _():
            dw_ref[...] = jnp.zeros_like(dw_ref)
            loss_ref[...] = jnp.zeros_like(loss_ref)

        dw_ref[...] += jnp.sum(dwt, axis=0, keepdims=True)
        part = 0.5 * jnp.sum(jnp.mean(err * err, axis=-1, keepdims=True), axis=0, keepdims=True)
        loss_ref[...] += jnp.broadcast_to(part, loss_ref.shape)

    return pl.pallas_call(
        body, name="loss_head", grid=(S // TR,),
        in_specs=[_row_spec(D), _row_spec(D), _vec_spec(D), _row_spec(D)],
        out_specs=[_row_spec(D), _row_spec(D), _vec_spec(D), _vec_spec(LANES)],
        out_shape=[jax.ShapeDtypeStruct((S, D), F32), jax.ShapeDtypeStruct((S, D), BF16),
                   jax.ShapeDtypeStruct((1, D), F32), jax.ShapeDtypeStruct((1, LANES), F32)],
        compiler_params=_cparams(),
    )(x1, y, w_post_mlp, target)


def _mid_bwd(dh2, x1, w_pre_mlp, dx2, mixed, w_post):
    def body(dh2_ref, x1_ref, wm_ref, dx2_ref, m_ref, wp_ref, dx1_ref, dm_ref, dwm_ref, dwp_ref):
        i = pl.program_id(0)
        dxa, dwm = _rms_bwd(x1_ref[...], wm_ref[...], dh2_ref[...])
        dx1 = dx2_ref[...] + dxa
        dx1_ref[...] = dx1
        dm, dwp = _rms_bwd(m_ref[...], wp_ref[...], dx1)
        dm_ref[...] = dm.astype(BF16)

        @pl.when(i == 0)
        def _():
            dwm_ref[...] = jnp.zeros_like(dwm_ref)
            dwp_ref[...] = jnp.zeros_like(dwp_ref)

        dwm_ref[...] += jnp.sum(dwm, axis=0, keepdims=True)
        dwp_ref[...] += jnp.sum(dwp, axis=0, keepdims=True)

    return pl.pallas_call(
        body, name="mid_bwd", grid=(S // TR,),
        in_specs=[_row_spec(D), _row_spec(D), _vec_spec(D), _row_spec(D), _row_spec(D), _vec_spec(D)],
        out_specs=[_row_spec(D), _row_spec(D), _vec_spec(D), _vec_spec(D)],
        out_shape=[jax.ShapeDtypeStruct((S, D), F32), jax.ShapeDtypeStruct((S, D), BF16),
                   jax.ShapeDtypeStruct((1, D), F32), jax.ShapeDtypeStruct((1, D), F32)],
        compiler_params=_cparams(),
    )(dh2, x1, w_pre_mlp, dx2, mixed, w_post)


def _pre_norm_bwd(dh, x, w, dx1):
    def body(dh_ref, x_ref, w_ref, dx1_ref, dx_ref, dw_ref):
        i = pl.program_id(0)
        dxa, dwt = _rms_bwd(x_ref[...], w_ref[...], dh_ref[...])
        dx_ref[...] = dx1_ref[...] + dxa

        @pl.when(i == 0)
        def _():
            dw_ref[...] = jnp.zeros_like(dw_ref)

        dw_ref[...] += jnp.sum(dwt, axis=0, keepdims=True)

    return pl.pallas_call(
        body, name="pre_norm_bwd", grid=(S // TR,),
        in_specs=[_row_spec(D), _row_spec(D), _vec_spec(D), _row_spec(D)], out_specs=[_row_spec(D), _vec_spec(D)],
        out_shape=[jax.ShapeDtypeStruct((S, D), F32), jax.ShapeDtypeStruct((1, D), F32)], compiler_params=_cparams(),
    )(dh, x, w, dx1)


BQ = 512
NQ = S // BQ
LANE_BETA, LANE_G = 8, 12


def _gate_lanes(shape):
    lane = lax.broadcasted_iota(jnp.int32, shape, 1)
    return lane < LANE_BETA, (lane >= LANE_BETA) & (lane < LANE_G), (lane >= LANE_G) & (lane < LANE_G + NGH)


def _gates(proj, bias_vec, alog_vec):
    def body(s_ref, b_ref, a_ref, o_ref, carry_ref):
        i = pl.program_id(0)

        @pl.when(i == 0)
        def _():
            carry_ref[...] = jnp.zeros_like(carry_ref)

        z = s_ref[...] + b_ref[...]
        tail = jnp.log(1.0 + jnp.exp(-jnp.abs(z)))
        sp = jnp.maximum(z, 0.0) + tail
        lf = jnp.minimum(z, 0.0) - tail
        r = lax.broadcasted_iota(jnp.int32, (BQ, BQ), 0)
        c = lax.broadcasted_iota(jnp.int32, (BQ, BQ), 1)
        tri = (c <= r).astype(F32)
        cum = _hdot(tri, lf) + carry_ref[...]
        carry_ref[...] = cum[BQ - 1:BQ, :]
        is_fox, is_beta, is_g = _gate_lanes(z.shape)
        o_ref[...] = jnp.where(is_fox, cum, jnp.where(is_beta, _sigmoid(z), jnp.where(is_g, -jnp.exp(a_ref[...]) * sp, 0.0)))

    return pl.pallas_call(
        body, name="gates", grid=(NQ,),
        in_specs=[pl.BlockSpec((BQ, LANES), lambda i: (i, BLK_SMALL)), _vec_spec(LANES), _vec_spec(LANES)],
        out_specs=pl.BlockSpec((BQ, LANES), lambda i: (i, 0)), out_shape=jax.ShapeDtypeStruct((S, LANES), F32),
        scratch_shapes=[pltpu.VMEM((1, LANES), F32)], compiler_params=_cparams(),
    )(proj, bias_vec, alog_vec)


def _gates_bwd(proj, bias_vec, alog_vec, dgates_gdn, dcum_fox, dproj):
    def body(s_ref, b_ref, a_ref, dg_ref, dc_ref, dproj_in, dproj_ref, red_ref, carry_ref):
        del dproj_in
        i = pl.program_id(0)

        @pl.when(i == 0)
        def _():
            carry_ref[...] = jnp.zeros_like(carry_ref)
            red_ref[...] = jnp.zeros_like(red_ref)

        z = s_ref[...] + b_ref[...]
        dg = dg_ref[...] + dc_ref[...]
        r = lax.broadcasted_iota(jnp.int32, (BQ, BQ), 0)
        c = lax.broadcasted_iota(jnp.int32, (BQ, BQ), 1)
        upper = (c >= r).astype(F32)
        dlf = _hdot(upper, dg) + carry_ref[...]
        carry_ref[...] = dlf[0:1, :]
        sig = _sigmoid(z)
        g_scale = -jnp.exp(a_ref[...])
        is_fox, is_beta, is_g = _gate_lanes(z.shape)
        ds = jnp.where(is_fox, dlf * (1.0 - sig), jnp.where(is_beta, dg * sig * (1.0 - sig), jnp.where(is_g, dg * g_scale * sig, 0.0)))
        dproj_ref[:, 0:LANES] = ds.astype(BF16)
        dproj_ref[:, LANES:2 * LANES] = jnp.zeros((BQ, LANES), BF16)
        dalog = jnp.where(is_g, dg * g_scale * _softplus(z), 0.0)
        sums = jnp.sum(ds, axis=0, keepdims=True)
        red_ref[0:1, :] += jnp.where(is_fox[0:1], sums, 0.0)
        red_ref[1:2, :] += pltpu.roll(jnp.where(is_g[0:1], sums, 0.0), LANES - LANE_G, 1)
        red_ref[2:3, :] += pltpu.roll(jnp.sum(dalog, axis=0, keepdims=True), LANES - LANE_G, 1)

    blk = pl.BlockSpec((BQ, LANES), lambda i: (NQ - 1 - i, 0))
    return pl.pallas_call(
        body, name="gates_bwd", grid=(NQ,),
        in_specs=[pl.BlockSpec((BQ, LANES), lambda i: (NQ - 1 - i, BLK_SMALL)), _vec_spec(LANES), _vec_spec(LANES), blk, blk,
                  pl.BlockSpec(memory_space=pl.ANY)],
        out_specs=[pl.BlockSpec((BQ, 2 * LANES), lambda i: (NQ - 1 - i, BLK_SMALL // 2)), pl.BlockSpec((8, LANES), lambda i: (0, 0))],
        out_shape=[jax.ShapeDtypeStruct((S, DPROJ_PAD), BF16), jax.ShapeDtypeStruct((8, LANES), F32)],
        input_output_aliases={5: 0},
        scratch_shapes=[pltpu.VMEM((1, LANES), F32)], compiler_params=_cparams(),
    )(proj, bias_vec, alog_vec, dgates_gdn, dcum_fox, dproj)


FOX_SCALE = FHD ** -0.5
FOX_PAIRS = 2
FOX_PAIRS_BWD = 2


def _head_mask(e):
    lane = lax.broadcasted_iota(jnp.int32, (1, LANES), 1)
    return (lane >= e * FHD) & (lane < (e + 1) * FHD)


def _lane_col(vals, index):
    lane = lax.broadcasted_iota(jnp.int32, vals.shape, 1)
    return jnp.sum(jnp.where(lane == index, vals, 0.0), axis=1, keepdims=True)


def _sublane_row(vals, index):
    row = lax.broadcasted_iota(jnp.int32, vals.shape, 0)
    return jnp.sum(jnp.where(row == index, vals, 0.0), axis=0, keepdims=True)


def _pair_cols(c0, c1):
    lane = lax.broadcasted_iota(jnp.int32, (c0.shape[0], 2), 1)
    return jnp.where(lane == 0, c0, c1)


def _split3(x):
    hi = x.astype(BF16).astype(F32)
    rest = x - hi
    mid = rest.astype(BF16).astype(F32)
    return hi, mid, (rest - mid).astype(BF16).astype(F32)


def _fox_operand(vals, e, cum, is_query):
    lane = lax.broadcasted_iota(jnp.int32, (1, LANES), 1)
    base = (1 - e) * FHD
    parts = _split3(cum)
    own = jnp.where(_head_mask(e), vals * FOX_SCALE if is_query else vals, 0.0)
    cum_at, ones_at = (base, base + 3) if is_query else (base + 3, base)
    sign = 1.0 if is_query else -1.0
    out = own + jnp.where((lane >= ones_at) & (lane < ones_at + 3), 1.0, 0.0)
    for i, part in enumerate(parts):
        out = out + jnp.where(lane == cum_at + i, sign * part, 0.0)
    return out.astype(BF16)


def _causal_block():
    return lax.broadcasted_iota(jnp.int32, (BQ, BQ), 1) <= lax.broadcasted_iota(jnp.int32, (BQ, BQ), 0)


def _head_rms(o, masks):
    o2 = o * o
    r = [lax.rsqrt(jnp.sum(jnp.where(mk, o2, 0.0), axis=1, keepdims=True) * (1.0 / FHD) + EPS) for mk in masks]
    return jnp.where(masks[0], r[0], r[1])


def _hosted(exchange):
    if exchange is None:
        return [], [], [], [], []
    return (exchange.inputs, [HBM] * len(exchange.inputs), [HBM] * len(exchange.out_shape), exchange.out_shape,
            exchange.sem_shapes())


def _fox_fwd(proj, gates, w2, exchange=None):
    ex_in, ex_in_specs, ex_out_specs, ex_out_shape, ex_scratch = _hosted(exchange)

    n_in = 3 * FOX_PAIRS + 2
    heads = [(pp, e) for pp in range(FOX_PAIRS) for e in range(2)]

    def body(*refs):
        qkv_refs, g_ref, w_ref = refs[:3 * FOX_PAIRS], refs[3 * FOX_PAIRS], refs[3 * FOX_PAIRS + 1]
        mix_ref, o_ref, lse_ref = refs[n_in + len(ex_in):n_in + 3 + len(ex_in)]
        ka_ref, vb_ref = refs[n_in + 3 + len(ex_in) + len(ex_out_shape):n_in + 5 + len(ex_in) + len(ex_out_shape)]
        ex_refs = refs[n_in:n_in + len(ex_in)] + refs[n_in + 3 + len(ex_in):n_in + 3 + len(ex_in) + len(ex_out_shape)] + refs[-2:]
        grp, qi = pl.program_id(0), pl.program_id(1)

        def head_index(pp, e):
            return 2 * (FOX_PAIRS * grp + pp) + e

        if exchange is not None:
            @pl.when((grp == 0) & (qi == 0))
            def _():
                exchange.start(*exchange.split(ex_refs))

        @pl.when(qi == 0)
        def _():
            gt = g_ref[...]
            for pp in range(FOX_PAIRS):
                kv = qkv_refs[3 * pp + 1][...]
                for e in range(2):
                    ka_ref[2 * pp + e] = _fox_operand(kv, e, _lane_col(gt, head_index(pp, e)), False)
                vb_ref[pp] = qkv_refs[3 * pp + 2][...].astype(BF16)

        masks = [_head_mask(0), _head_mask(1)]
        gt = g_ref[pl.ds(pl.multiple_of(qi * BQ, BQ), BQ), :]
        qs = [_fox_operand(qkv_refs[3 * pp][...], e, _lane_col(gt, head_index(pp, e)), True) for pp, e in heads]
        n = range(len(heads))

        def block(kj, carry, diagonal):
            rows = pl.ds(pl.multiple_of(kj * BQ, BQ), BQ)
            s = [_dot(qs[i], ka_ref[i, rows, :], 1, 1) for i in n]
            if diagonal:
                s = [jnp.where(_causal_block(), s[i], -jnp.inf) for i in n]
            m_new = [jnp.maximum(carry[i][0], jnp.max(s[i], axis=-1, keepdims=True)) for i in n]
            p = [jnp.exp(s[i] - m_new[i]) for i in n]
            alpha = [jnp.exp(carry[i][0] - m_new[i]) for i in n]
            l_new = [alpha[i] * carry[i][1] + jnp.sum(p[i], axis=-1, keepdims=True) for i in n]
            pv = [_dot(p[i], vb_ref[heads[i][0], rows, :]) for i in n]
            return tuple((m_new[i], l_new[i], alpha[i] * carry[i][2] + pv[i]) for i in n)

        one = (jnp.full((BQ, 1), -jnp.inf, F32), jnp.zeros((BQ, 1), F32), jnp.zeros((BQ, LANES), F32))
        below = lax.fori_loop(0, qi, lambda kj, carry: block(kj, carry, False), (one,) * len(heads))
        done = block(qi, below, True)
        for pp in range(FOX_PAIRS):
            (m0, l0, a0), (m1, l1, a1) = done[2 * pp], done[2 * pp + 1]
            o = jnp.where(masks[0], a0 / l0, a1 / l1)
            cols = slice(pp * LANES, (pp + 1) * LANES)
            o_ref[:, cols] = o
            mix_ref[:, cols] = (o * _head_rms(o, masks) * w_ref[...]).astype(BF16)
            lse_ref[pp] = _pair_cols(m0 + jnp.log(l0), m1 + jnp.log(l1))

        if exchange is not None:
            @pl.when((grp == NPAIR // FOX_PAIRS // 2) & (qi == 0))
            def _():
                exchange.middle(*exchange.split(ex_refs))

            @pl.when((grp == NPAIR // FOX_PAIRS - 1) & (qi == NQ - 1))
            def _():
                exchange.rest(*exchange.split(ex_refs))

    qkv_specs = []
    for pp in range(FOX_PAIRS):
        qkv_specs.append(pl.BlockSpec((BQ, LANES), lambda g, i, pp=pp: (i, 3 * (FOX_PAIRS * g + pp))))
        qkv_specs.append(pl.BlockSpec((S, LANES), lambda g, i, pp=pp: (0, 3 * (FOX_PAIRS * g + pp) + 1)))
        qkv_specs.append(pl.BlockSpec((S, LANES), lambda g, i, pp=pp: (0, 3 * (FOX_PAIRS * g + pp) + 2)))
    blk = pl.BlockSpec((BQ, FOX_PAIRS * LANES), lambda g, i: (i, g))
    res = pl.pallas_call(
        body, name="fox_fwd", grid=(NPAIR // FOX_PAIRS, NQ),
        in_specs=qkv_specs + [pl.BlockSpec((S, LANES), lambda g, i: (0, 0)), pl.BlockSpec((1, LANES), lambda g, i: (0, 0))]
        + ex_in_specs,
        out_specs=[blk, blk, pl.BlockSpec((FOX_PAIRS, BQ, 2), lambda g, i: (g, i, 0))] + ex_out_specs,
        out_shape=[jax.ShapeDtypeStruct((S, D), BF16), jax.ShapeDtypeStruct((S, DFOX), F32),
                   jax.ShapeDtypeStruct((NPAIR, S, 2), F32)] + ex_out_shape,
        scratch_shapes=[pltpu.VMEM((2 * FOX_PAIRS, S, LANES), BF16), pltpu.VMEM((FOX_PAIRS, S, LANES), BF16)] + ex_scratch,
        compiler_params=_cparams(),
    )(*([proj] * (3 * FOX_PAIRS)), gates, w2, *ex_in)
    return res[0], res[1], res[2], res[3:]


def _fox_norm_bwd(o, dmix, w2, exchange=None):
    ex_in, ex_in_specs, ex_out_specs, ex_out_shape, ex_scratch = _hosted(exchange)

    def body(*refs):
        o_ref, g_ref, w_ref = refs[:3]
        do_ref, dl_ref, dw_ref = refs[3 + len(ex_in):6 + len(ex_in)]
        ex_refs = refs[3:3 + len(ex_in)] + refs[6 + len(ex_in):]
        hp, qi = pl.program_id(0), pl.program_id(1)

        if exchange is not None:
            @pl.when((hp == 0) & (qi == 0))
            def _():
                exchange.start(*exchange.split(ex_refs))

        masks = [_head_mask(0), _head_mask(1)]
        ov = o_ref[...]
        g = g_ref[...]
        r = _head_rms(ov, masks)
        gw = g * w_ref[...]
        gwo = gw * ov
        mean = [jnp.sum(jnp.where(mk, gwo, 0.0), axis=1, keepdims=True) * (1.0 / FHD) for mk in masks]
        do = r * gw - ov * (r * r * r) * jnp.where(masks[0], mean[0], mean[1])
        do_ref[...] = do.astype(BF16)
        doo = do * ov
        dl_ref[...] = _pair_cols(*[jnp.sum(jnp.where(mk, doo, 0.0), axis=1, keepdims=True) for mk in masks])

        @pl.when((hp == 0) & (qi == 0))
        def _():
            dw_ref[...] = jnp.zeros_like(dw_ref)

        dw_ref[...] += jnp.sum(g * ov * r, axis=0, keepdims=True)

        @pl.when((hp == NPAIR - 1) & (qi == NQ - 1))
        def _():
            dw = dw_ref[...]
            dw_ref[...] = dw + pltpu.roll(dw, FHD, 1)
            if exchange is not None:
                exchange.finish(*exchange.split(ex_refs))

    blk = pl.BlockSpec((BQ, LANES), lambda hp, i: (i, hp))
    vec = pl.BlockSpec((1, LANES), lambda hp, i: (0, 0))
    res = pl.pallas_call(
        body, name="fox_norm_bwd", grid=(NPAIR, NQ), in_specs=[blk, blk, vec] + ex_in_specs,
        out_specs=[blk, pl.BlockSpec((None, BQ, 2), lambda hp, i: (hp, i, 0)), vec] + ex_out_specs,
        out_shape=[jax.ShapeDtypeStruct((S, DFOX), BF16), jax.ShapeDtypeStruct((NPAIR, S, 2), F32),
                   jax.ShapeDtypeStruct((1, LANES), F32)] + ex_out_shape,
        scratch_shapes=ex_scratch, compiler_params=_cparams(),
    )(o, dmix, w2, *ex_in)
    return res[0], res[1], res[2], res[3:]


def _fox_bwd(proj, do, gates, lse, delta, exchange=None):
    ex_in, ex_in_specs, ex_out_specs, ex_out_shape, ex_scratch = _hosted(exchange)

    pg = FOX_PAIRS_BWD
    n_in = 3 * pg + 4
    heads = [(pp, e) for pp in range(pg) for e in range(2)]

    def body(*refs):
        qkv_refs = refs[:3 * pg]
        do_ref, g_ref, lse_ref, dl_ref = refs[3 * pg:n_in]
        dproj_ref, dc_ref = refs[n_in + len(ex_in):n_in + 2 + len(ex_in)]
        qa_ref, dq_ref = refs[n_in + 2 + len(ex_in) + len(ex_out_shape):n_in + 4 + len(ex_in) + len(ex_out_shape)]
        ex_refs = refs[n_in:n_in + len(ex_in)] + refs[n_in + 2 + len(ex_in):n_in + 2 + len(ex_in) + len(ex_out_shape)] + refs[-2:]
        grp, kj = pl.program_id(0), pl.program_id(1)

        def head_index(pp, e):
            return 2 * (pg * grp + pp) + e

        if exchange is not None:
            @pl.when((grp == 0) & (kj == 0))
            def _():
                exchange.start(*exchange.split(ex_refs))

        @pl.when(kj == 0)
        def _():
            gt = g_ref[...]
            for pp in range(pg):
                qv = qkv_refs[3 * pp][...]
                for e in range(2):
                    qa_ref[2 * pp + e] = _fox_operand(qv, e, _lane_col(gt, head_index(pp, e)), True)
            dq_ref[...] = jnp.zeros_like(dq_ref)

        @pl.when((grp == 0) & (kj == 0))
        def _():
            dc_ref[...] = jnp.zeros_like(dc_ref)

        masks = [_head_mask(0), _head_mask(1)]
        krows = pl.ds(pl.multiple_of(kj * BQ, BQ), BQ)
        gk = g_ref[krows, :]
        kas = [_fox_operand(qkv_refs[3 * pp + 1][...], e, _lane_col(gk, head_index(pp, e)), False) for pp, e in heads]
        vbs = [qkv_refs[3 * pp + 2][...].astype(BF16) for pp in range(pg)]
        lane = lax.broadcasted_iota(jnp.int32, (BQ, LANES), 1)
        n = range(len(heads))

        def block(qi, carry, diagonal):
            dks, dvs, css = carry
            rows = pl.ds(pl.multiple_of(qi * BQ, BQ), BQ)
            qa = [qa_ref[i, rows, :] for i in n]
            s = [_dot(qa[i], kas[i], 1, 1) for i in n]
            if diagonal:
                s = [jnp.where(_causal_block(), s[i], -jnp.inf) for i in n]
            dov = [do_ref[rows, pp * LANES:(pp + 1) * LANES] for pp in range(pg)]
            doe = [jnp.where(masks[e], dov[pp], jnp.zeros_like(dov[pp])) for pp, e in heads]
            lse2 = [lse_ref[pp, rows, :] for pp in range(pg)]
            dl2 = [dl_ref[pp, rows, :] for pp in range(pg)]
            p = [jnp.exp(s[i] - _lane_col(lse2[heads[i][0]], heads[i][1])) for i in n]
            dp = [_dot(doe[i], vbs[heads[i][0]], 1, 1) for i in n]
            ds = [p[i] * (dp[i] - _lane_col(dl2[heads[i][0]], heads[i][1])) for i in n]
            dv_part = [_dot(p[i], doe[i], 0, 0) for i in n]
            dk_part = [_dot(ds[i], jnp.where(masks[heads[i][1]], qa[i], jnp.zeros_like(qa[i])), 0, 0) for i in n]
            dq_part = [jnp.where(masks[heads[i][1]], _dot(ds[i], kas[i]), 0.0) for i in n]
            css = tuple(css[i] + jnp.sum(ds[i], axis=0, keepdims=True) for i in n)
            dc = jnp.zeros((BQ, LANES), F32)
            for i in n:
                dc = dc + jnp.where(lane == head_index(*heads[i]), jnp.sum(ds[i], axis=1, keepdims=True), 0.0)
            for pp in range(pg):
                dq_ref[pp, rows, :] += (dq_part[2 * pp] + dq_part[2 * pp + 1]) * FOX_SCALE
            dc_ref[rows, :] += dc
            dks = tuple(dks[pp] + dk_part[2 * pp] + dk_part[2 * pp + 1] for pp in range(pg))
            dvs = tuple(dvs[pp] + dv_part[2 * pp] + dv_part[2 * pp + 1] for pp in range(pg))
            return dks, dvs, css

        zero = jnp.zeros((BQ, LANES), F32)
        first = block(kj, ((zero,) * pg, (zero,) * pg, (jnp.zeros((1, BQ), F32),) * len(heads)), True)
        dks, dvs, css = lax.fori_loop(kj + 1, NQ, lambda qi, carry: block(qi, carry, False), first)
        r = lax.broadcasted_iota(jnp.int32, (BQ, BQ), 0)
        c = lax.broadcasted_iota(jnp.int32, (BQ, BQ), 1)
        dcol = jnp.zeros((BQ, LANES), F32)
        for i in n:
            col = jnp.sum(jnp.where(r == c, css[i], 0.0), axis=1, keepdims=True)
            dcol = dcol + jnp.where(lane == head_index(*heads[i]), col, 0.0)
        dc_ref[krows, :] -= dcol
        for pp in range(pg):
            base = 3 * pp * LANES
            dproj_ref[krows, base + LANES:base + 2 * LANES] = dks[pp].astype(BF16)
            dproj_ref[krows, base + 2 * LANES:base + 3 * LANES] = dvs[pp].astype(BF16)

        @pl.when(kj == NQ - 1)
        def _():
            for pp in range(pg):
                dproj_ref[:, 3 * pp * LANES:(3 * pp + 1) * LANES] = dq_ref[pp].astype(BF16)

        if exchange is not None:
            @pl.when((grp == NPAIR // pg // 2) & (kj == 0))
            def _():
                exchange.middle(*exchange.split(ex_refs))

            @pl.when((grp == NPAIR // pg - 1) & (kj == NQ - 1))
            def _():
                exchange.rest(*exchange.split(ex_refs))

    qkv_specs = []
    for pp in range(pg):
        qkv_specs.append(pl.BlockSpec((S, LANES), lambda g, j, pp=pp: (0, 3 * (pg * g + pp))))
        qkv_specs.append(pl.BlockSpec((BQ, LANES), lambda g, j, pp=pp: (j, 3 * (pg * g + pp) + 1)))
        qkv_specs.append(pl.BlockSpec((BQ, LANES), lambda g, j, pp=pp: (j, 3 * (pg * g + pp) + 2)))
    pair = pl.BlockSpec((pg, S, 2), lambda g, j: (g, 0, 0))
    res = pl.pallas_call(
        body, name="fox_bwd", grid=(NPAIR // pg, NQ),
        in_specs=qkv_specs + [pl.BlockSpec((S, pg * LANES), lambda g, j: (0, g)), pl.BlockSpec((S, LANES), lambda g, j: (0, 0)),
                              pair, pair] + ex_in_specs,
        out_specs=[pl.BlockSpec((S, 3 * pg * LANES), lambda g, j: (0, g)), pl.BlockSpec((S, LANES), lambda g, j: (0, 0))]
        + ex_out_specs,
        out_shape=[jax.ShapeDtypeStruct((S, DPROJ_PAD), BF16), jax.ShapeDtypeStruct((S, LANES), F32)] + ex_out_shape,
        scratch_shapes=[pltpu.VMEM((2 * pg, S, LANES), BF16), pltpu.VMEM((pg, S, LANES), F32)] + ex_scratch,
        compiler_params=_cparams(),
    )(*([proj] * (3 * pg)), do, gates, lse, delta, *ex_in)
    return res[0], res[1], res[2:]


NQKV = 3 * NGH
GDN_QSCALE = GHD ** -0.5


def _shift_down(x, s):
    if s == 0:
        return x
    row = lax.broadcasted_iota(jnp.int32, x.shape, 0)
    return jnp.where(row >= s, pltpu.roll(x, s, 0), 0.0)


def _shift_up(x, s):
    if s == 0:
        return x
    n = x.shape[0]
    row = lax.broadcasted_iota(jnp.int32, x.shape, 0)
    return jnp.where(row < n - s, pltpu.roll(x, n - s, 0), 0.0)


def _conv_pre(xv, wv):
    pre = xv * wv[CONV_K - 1:CONV_K, :]
    for j in range(CONV_K - 1):
        pre = pre + _shift_down(xv, CONV_K - 1 - j) * wv[j:j + 1, :]
    return pre


def _l2_factors(b):
    return b < 2 * NGH, jnp.where(b < NGH, GDN_QSCALE, 1.0)


def _gdn_pre(proj, conv_w):
    def body(x_ref, w_ref, o_ref):
        b = pl.program_id(0)
        c = _silu(_conv_pre(x_ref[...], w_ref[...]))
        normed, scale = _l2_factors(b)
        rs = lax.rsqrt(jnp.sum(c * c, axis=-1, keepdims=True) + EPS)
        o_ref[...] = c * jnp.where(normed, rs, 1.0) * scale

    return pl.pallas_call(
        body, name="gdn_pre", grid=(NQKV,),
        in_specs=[pl.BlockSpec((S, GHD), lambda b: (0, BLK_GDN + b)), pl.BlockSpec((CONV_K, GHD), lambda b: (0, b))],
        out_specs=pl.BlockSpec((S, GHD), lambda b: (0, b)),
        out_shape=jax.ShapeDtypeStruct((S, NQKV * GHD), F32), compiler_params=_cparams(),
    )(proj, conv_w)


def _gdn_pre_bwd(proj, conv_w, dqkv, dproj):
    def body(x_ref, w_ref, dy_ref, dproj_in, dx_ref, dw_ref):
        del dproj_in
        b = pl.program_id(0)
        xv = x_ref[...]
        wv = w_ref[...]
        pre = _conv_pre(xv, wv)
        sig = _sigmoid(pre)
        c = pre * sig
        normed, scale = _l2_factors(b)
        g = dy_ref[...] * scale
        rs = lax.rsqrt(jnp.sum(c * c, axis=-1, keepdims=True) + EPS)
        dc_n = rs * g - c * (rs * rs * rs) * jnp.sum(g * c, axis=-1, keepdims=True)
        dc = jnp.where(normed, dc_n, g)
        dpre = dc * sig * (1.0 + pre * (1.0 - sig))
        dx = dpre * wv[CONV_K - 1:CONV_K, :]
        for j in range(CONV_K - 1):
            dx = dx + _shift_up(dpre, CONV_K - 1 - j) * wv[j:j + 1, :]
        dx_ref[...] = dx.astype(BF16)
        for j in range(CONV_K):
            dw_ref[j:j + 1, :] = jnp.sum(dpre * _shift_down(xv, CONV_K - 1 - j), axis=0, keepdims=True)

    return pl.pallas_call(
        body, name="gdn_pre_bwd", grid=(NQKV,),
        in_specs=[pl.BlockSpec((S, GHD), lambda b: (0, BLK_GDN + b)), pl.BlockSpec((CONV_K, GHD), lambda b: (0, b)),
                  pl.BlockSpec((None, S, GHD), lambda b: (b // NGH, 0, b % NGH)), pl.BlockSpec(memory_space=pl.ANY)],
        out_specs=[pl.BlockSpec((S, GHD), lambda b: (0, BLK_GDN + b)), pl.BlockSpec((CONV_K, GHD), lambda b: (0, b))],
        out_shape=[jax.ShapeDtypeStruct((S, DPROJ_PAD), BF16), jax.ShapeDtypeStruct((CONV_K, NQKV * GHD), F32)],
        input_output_aliases={3: 0}, compiler_params=_cparams(),
    )(proj, conv_w, dqkv, dproj)


CB = 16
NCB = NCH // CB


def _chunk_prep(qs, ks, vs, gcols, bcols, t_saved=None):
    n = range(len(qs))
    r = lax.broadcasted_iota(jnp.int32, (CHUNK, CHUNK), 0)
    c = lax.broadcasted_iota(jnp.int32, (CHUNK, CHUNK), 1)
    incl = c <= r
    eye = (r == c).astype(F32)
    grow = [jnp.sum(gcols[i] * eye, axis=0, keepdims=True) for i in n]
    gc_col = [jnp.sum(jnp.where(incl, grow[i], 0.0), axis=1, keepdims=True) for i in n]
    gc_row = [jnp.sum(jnp.where(r <= c, gcols[i], 0.0), axis=0, keepdims=True) for i in n]
    decay = [jnp.exp(jnp.where(incl, gc_col[i] - gc_row[i], -jnp.inf)) for i in n]
    kb = [ks[i] * bcols[i] for i in n]
    vb = [vs[i] * bcols[i] for i in n]
    kk = [_mm_nt(kb[i], ks[i]) for i in n]
    m = [jnp.where(c < r, kk[i] * decay[i], 0.0) for i in n]
    if t_saved is None:
        t_inv = [eye - m[i] for i in n]
        p = [_dot3(m[i], m[i]) for i in n]
        for step in range(5):
            t_inv = [t_inv[i] + _dot3(t_inv[i], p[i]) for i in n]
            if step < 4:
                p = [_dot3(p[i], p[i]) for i in n]
    else:
        t_inv = [_saved_inverse(m[i], t_saved[i]) for i in n]
    egc = [jnp.exp(gc_col[i]) for i in n]
    u = [_mm_nn(t_inv[i], vb[i]) for i in n]
    w = [_mm_nn(t_inv[i], kb[i] * egc[i]) for i in n]
    qk = [_mm_nt(qs[i], ks[i]) for i in n]
    gc_last = [gc_col[i][CHUNK - 1:CHUNK, :] for i in n]
    return [(u[i], w[i], qk[i] * decay[i], qs[i] * egc[i], ks[i] * jnp.exp(gc_last[i] - gc_col[i]), jnp.exp(gc_last[i]),
             t_inv[i]) for i in n]


def _prep_specs():
    rows = CB * CHUNK
    qs = pl.BlockSpec((rows, GHD), lambda i, h: (i, h))
    ks = pl.BlockSpec((rows, GHD), lambda i, h: (i, NGH + h))
    vs = pl.BlockSpec((rows, GHD), lambda i, h: (i, 2 * NGH + h))
    gs = pl.BlockSpec((rows, LANES), lambda i, h: (i, 0))
    a_s = pl.BlockSpec((None, rows, CHUNK), lambda i, h: (h, i, 0))
    gl_s = pl.BlockSpec((None, CB, 1, LANES), lambda i, h: (h, i, 0, 0))
    return qs, ks, vs, gs, a_s, gl_s


def _gdn_prep(qkv, gates, exchange=None):
    ex_in, ex_in_specs, ex_out_specs, ex_out_shape, ex_scratch = _hosted(exchange)

    def body(*refs):
        q_ref, k_ref, v_ref, g_ref = refs[:4]
        u_ref, w_ref, qd_ref, kd_ref, a_ref, gl_ref, t_ref = refs[4 + len(ex_in):11 + len(ex_in)]
        ex_refs = refs[4:4 + len(ex_in)] + refs[11 + len(ex_in):]
        h = pl.program_id(1)

        if exchange is not None:
            @pl.when((pl.program_id(0) == 0) & (h == 0))
            def _():
                exchange.start(*exchange.split(ex_refs))

        chunks = [pl.ds(cidx * CHUNK, CHUNK) for cidx in range(CB)]
        gts = [g_ref[rows, :] for rows in chunks]
        outs = _chunk_prep([q_ref[rows, :] for rows in chunks], [k_ref[rows, :] for rows in chunks],
                           [v_ref[rows, :] for rows in chunks], [_lane_col(gt, LANE_G + h) for gt in gts],
                           [_lane_col(gt, LANE_BETA + h) for gt in gts])
        for cidx, rows in enumerate(chunks):
            u, w, a, qd, kd, gl, t_inv = outs[cidx]
            u_ref[rows, :] = u
            w_ref[rows, :] = w
            qd_ref[rows, :] = qd
            kd_ref[rows, :] = kd
            a_ref[rows, :] = a
            t_ref[rows, :] = t_inv
            gl_ref[cidx] = jnp.broadcast_to(gl, (1, LANES))

        if exchange is not None:
            @pl.when((pl.program_id(0) == NCB // 2) & (h == 0))
            def _():
                exchange.middle(*exchange.split(ex_refs))

            @pl.when((pl.program_id(0) == NCB - 1) & (h == NGH - 1))
            def _():
                exchange.rest(*exchange.split(ex_refs))

    qs, ks, vs, gs, a_s, gl_s = _prep_specs()
    tok = jax.ShapeDtypeStruct((S, DGDN), F32)
    sq = jax.ShapeDtypeStruct((NGH, S, CHUNK), F32)
    res = pl.pallas_call(
        body, name="gdn_prep", grid=(NCB, NGH), in_specs=[qs, ks, vs, gs] + ex_in_specs,
        out_specs=[qs, qs, qs, qs, a_s, gl_s, a_s] + ex_out_specs,
        out_shape=[tok, tok, tok, tok, sq, jax.ShapeDtypeStruct((NGH, NCH, 1, LANES), F32), sq] + ex_out_shape,
        scratch_shapes=ex_scratch, compiler_params=_cparams(),
    )(qkv, qkv, qkv, gates, *ex_in)
    return res[:7], res[7:]


def _gdn_prep_bwd(qkv, gates, t_inv, du, dw, dqd, dkd, da, dgl, exchange=None):
    ex_in, ex_in_specs, ex_out_specs, ex_out_shape, ex_scratch = _hosted(exchange)

    def body(*refs):
        q_ref, k_ref, v_ref, g_ref, t_ref, du_ref, dw_ref, dqd_ref, dkd_ref, da_ref, dgl_ref = refs[:11]
        dqkv_ref, dg_ref = refs[11 + len(ex_in):13 + len(ex_in)]
        ex_refs = refs[11:11 + len(ex_in)] + refs[13 + len(ex_in):]
        h = pl.program_id(1)

        if exchange is not None:
            @pl.when((pl.program_id(0) == 0) & (h == 0))
            def _():
                exchange.start(*exchange.split(ex_refs))

        @pl.when(h == 0)
        def _():
            dg_ref[...] = jnp.zeros_like(dg_ref)

        lane = lax.broadcasted_iota(jnp.int32, (CHUNK, LANES), 1)
        chunks = [pl.ds(cidx * CHUNK, CHUNK) for cidx in range(CB)]
        gts = [g_ref[rows, :] for rows in chunks]
        t_saved = [t_ref[rows, :] for rows in chunks]
        _, vjp = jax.vjp(lambda *args: [o[:6] for o in _chunk_prep(*args, t_saved=t_saved)],
                         [q_ref[rows, :] for rows in chunks], [k_ref[rows, :] for rows in chunks],
                         [v_ref[rows, :] for rows in chunks], [_lane_col(gt, LANE_G + h) for gt in gts],
                         [_lane_col(gt, LANE_BETA + h) for gt in gts])
        dqs, dks, dvs, dgcs, dbcs = vjp([(du_ref[rows, :], dw_ref[rows, :], da_ref[rows, :], dqd_ref[rows, :],
                                          dkd_ref[rows, :], dgl_ref[cidx][:, 0:1]) for cidx, rows in enumerate(chunks)])
        for cidx, rows in enumerate(chunks):
            dq, dk, dv, dgc, dbc = dqs[cidx], dks[cidx], dvs[cidx], dgcs[cidx], dbcs[cidx]
            dqkv_ref[0, rows, :] = dq
            dqkv_ref[1, rows, :] = dk
            dqkv_ref[2, rows, :] = dv
            dg_ref[rows, :] += jnp.where(lane == LANE_G + h, dgc, 0.0) + jnp.where(lane == LANE_BETA + h, dbc, 0.0)

        if exchange is not None:
            @pl.when((pl.program_id(0) == NCB - 1) & (h == NGH - 1))
            def _():
                exchange.finish(*exchange.split(ex_refs))

    qs, ks, vs, gs, a_s, gl_s = _prep_specs()
    res = pl.pallas_call(
        body, name="gdn_prep_bwd", grid=(NCB, NGH), in_specs=[qs, ks, vs, gs, a_s, qs, qs, qs, qs, a_s, gl_s] + ex_in_specs,
        out_specs=[pl.BlockSpec((3, CB * CHUNK, GHD), lambda i, h: (0, i, h)), gs] + ex_out_specs,
        out_shape=[jax.ShapeDtypeStruct((3, S, DGDN), F32), jax.ShapeDtypeStruct((S, LANES), F32)] + ex_out_shape,
        scratch_shapes=ex_scratch, compiler_params=_cparams(),
    )(qkv, qkv, qkv, gates, t_inv, du, dw, dqd, dkd, da, dgl, *ex_in)
    return res[0], res[1], res[2:]


def _scan_specs(nh, parts, reverse):
    wide, rows, chunks = nh * GHD, S // parts, NCH // parts

    def part(p):
        return parts - 1 - p if reverse else p

    hs = pl.BlockSpec((rows, wide), lambda g, p: (part(p), g))
    a_s = pl.BlockSpec((nh, rows, CHUNK), lambda g, p: (g, part(p), 0))
    gl_s = pl.BlockSpec((nh, chunks, 1, LANES), lambda g, p: (g, part(p), 0, 0))
    st_s = pl.BlockSpec((nh, chunks, GHD, GHD), lambda g, p: (g, part(p), 0, 0))
    gz_s = pl.BlockSpec((rows, wide), lambda g, p: (part(p), BLK_GZ // nh + g))
    mix_s = pl.BlockSpec((rows, wide), lambda g, p: (part(p), NPAIR // nh + g))
    return hs, a_s, gl_s, st_s, gz_s, mix_s


def _head_cols(hh):
    return slice(hh * GHD, (hh + 1) * GHD)


SCAN_HEADS, SCAN_PARTS = 4, 2
SCAN_HEADS_BWD, SCAN_PARTS_BWD = 2, 2


def _gdn_scan(u, w, qd, kd, a, gl, proj, w_norm, mix):
    heads = range(SCAN_HEADS)

    def body(u_ref, w_ref, qd_ref, kd_ref, a_ref, gl_ref, z_ref, wn_ref, mix_in, mix_ref, o_ref, st_ref, carry_ref):
        del mix_in

        @pl.when(pl.program_id(1) == 0)
        def _():
            carry_ref[...] = jnp.zeros_like(carry_ref)

        def step(ci, states):
            rows = pl.ds(pl.multiple_of(ci * CHUNK, CHUNK), CHUNK)
            for hh in heads:
                st_ref[hh, ci] = states[hh]
            ws = [_dot(w_ref[rows, _head_cols(hh)], states[hh]) for hh in heads]
            qs = [_dot(qd_ref[rows, _head_cols(hh)], states[hh]) for hh in heads]
            vn = [u_ref[rows, _head_cols(hh)] - ws[hh] for hh in heads]
            av = [_dot(a_ref[hh, rows, :], vn[hh]) for hh in heads]
            kv = [_dot(kd_ref[rows, _head_cols(hh)], vn[hh], 0, 0) for hh in heads]
            for hh in heads:
                o_ref[rows, _head_cols(hh)] = qs[hh] + av[hh]
            return tuple(states[hh] * gl_ref[hh, ci] + kv[hh] for hh in heads)

        last = lax.fori_loop(0, NCH // SCAN_PARTS, step, tuple(carry_ref[hh] for hh in heads))
        for hh in heads:
            carry_ref[hh] = last[hh]
            ov = o_ref[:, _head_cols(hh)]
            mix_ref[:, _head_cols(hh)] = (ov * _rms_scale(ov) * wn_ref[...] * _silu(z_ref[:, _head_cols(hh)])).astype(BF16)

    hs, a_s, gl_s, st_s, gz_s, mix_s = _scan_specs(SCAN_HEADS, SCAN_PARTS, False)
    return pl.pallas_call(
        body, name="gdn_scan", grid=(NGH // SCAN_HEADS, SCAN_PARTS),
        in_specs=[hs, hs, hs, hs, a_s, gl_s, gz_s, pl.BlockSpec((1, GHD), lambda g, p: (0, 0)),
                  pl.BlockSpec(memory_space=pl.ANY)],
        out_specs=[mix_s, hs, st_s],
        out_shape=[jax.ShapeDtypeStruct((S, D), BF16), jax.ShapeDtypeStruct((S, DGDN), F32),
                   jax.ShapeDtypeStruct((NGH, NCH, GHD, GHD), F32)],
        input_output_aliases={8: 0}, scratch_shapes=[pltpu.VMEM((SCAN_HEADS, GHD, GHD), F32)], compiler_params=_cparams(),
    )(u, w, qd, kd, a, gl, proj, w_norm, mix)


def _gdn_scan_bwd(dmix, o, proj, w_norm, u, w, qd, kd, a, gl, states, dproj, exchange=None):
    ex_in, ex_in_specs, ex_out_specs, ex_out_shape, ex_scratch = _hosted(exchange)
    groups = NGH // SCAN_HEADS_BWD

    def body(*refs):
        dy_ref, o_ref, z_ref, wn_ref, u_ref, w_ref, qd_ref, kd_ref, a_ref, gl_ref, st_ref = refs[:11]
        dz_ref, du_ref, dw_ref, dqd_ref, dkd_ref, da_ref, dgl_ref, dwn_ref = refs[12 + len(ex_in):20 + len(ex_in)]
        do_ref, carry_ref = refs[20 + len(ex_in) + len(ex_out_shape):22 + len(ex_in) + len(ex_out_shape)]
        ex_refs = refs[12:12 + len(ex_in)] + refs[20 + len(ex_in):20 + len(ex_in) + len(ex_out_shape)] + refs[-2:]
        heads = range(SCAN_HEADS_BWD)
        chunks = NCH // SCAN_PARTS_BWD

        if exchange is not None:
            @pl.when((pl.program_id(0) == 0) & (pl.program_id(1) == 0))
            def _():
                exchange.start(*exchange.split(ex_refs))

        @pl.when((pl.program_id(0) == 0) & (pl.program_id(1) == 0))
        def _():
            dwn_ref[...] = jnp.zeros_like(dwn_ref)

        @pl.when(pl.program_id(1) == 0)
        def _():
            carry_ref[...] = jnp.zeros_like(carry_ref)

        wn = wn_ref[...]
        for hh in heads:
            c = _head_cols(hh)
            ov = o_ref[:, c]
            zv = z_ref[:, c]
            g = dy_ref[:, c]
            sig = _sigmoid(zv)
            dz_ref[:, c] = (g * (ov * _rms_scale(ov) * wn) * sig * (1.0 + zv * (1.0 - sig))).astype(BF16)
            do, dwt = _rms_bwd(ov, wn, g * zv * sig)
            do_ref[:, c] = do
            dwn_ref[...] += jnp.sum(dwt, axis=0, keepdims=True)

        def step(t, dstates):
            ci = chunks - 1 - t
            rows = pl.ds(pl.multiple_of(ci * CHUNK, CHUNK), CHUNK)
            cols = [_head_cols(hh) for hh in heads]
            state = [st_ref[hh, ci] for hh in heads]
            dov = [do_ref[rows, cols[hh]] for hh in heads]
            wv = [w_ref[rows, cols[hh]] for hh in heads]
            ws = [_dot(wv[hh], state[hh]) for hh in heads]
            adov = [_dot(a_ref[hh, rows, :], dov[hh], 0, 0) for hh in heads]
            kds = [_dot(kd_ref[rows, cols[hh]], dstates[hh]) for hh in heads]
            dqd = [_dot(dov[hh], state[hh], 1, 1) for hh in heads]
            qdo = [_dot(qd_ref[rows, cols[hh]], dov[hh], 0, 0) for hh in heads]
            vn = [u_ref[rows, cols[hh]] - ws[hh] for hh in heads]
            dvn = [adov[hh] + kds[hh] for hh in heads]
            da = [_dot(dov[hh], vn[hh], 1, 1) for hh in heads]
            dkd = [_dot(vn[hh], dstates[hh], 1, 1) for hh in heads]
            dwv = [_dot(dvn[hh], state[hh], 1, 1) for hh in heads]
            wdv = [_dot(wv[hh], dvn[hh], 0, 0) for hh in heads]
            for hh in heads:
                da_ref[hh, rows, :] = da[hh]
                dqd_ref[rows, cols[hh]] = dqd[hh]
                dkd_ref[rows, cols[hh]] = dkd[hh]
                dgl = jnp.sum(jnp.sum(dstates[hh] * state[hh], axis=1, keepdims=True), axis=0, keepdims=True)
                dgl_ref[hh, ci] = jnp.broadcast_to(dgl, (1, LANES))
                du_ref[rows, cols[hh]] = dvn[hh]
                dw_ref[rows, cols[hh]] = -dwv[hh]
            return tuple(dstates[hh] * gl_ref[hh, ci] + qdo[hh] - wdv[hh] for hh in heads)

        last = lax.fori_loop(0, chunks, step, tuple(carry_ref[hh] for hh in heads))
        for hh in heads:
            carry_ref[hh] = last[hh]

        if exchange is not None:
            @pl.when((pl.program_id(0) == groups - 1) & (pl.program_id(1) == SCAN_PARTS_BWD - 1))
            def _():
                exchange.finish(*exchange.split(ex_refs))

    hs, a_s, gl_s, st_s, gz_s, mix_s = _scan_specs(SCAN_HEADS_BWD, SCAN_PARTS_BWD, True)
    vec = pl.BlockSpec((1, GHD), lambda g, p: (0, 0))
    tok = jax.ShapeDtypeStruct((S, DGDN), F32)
    res = pl.pallas_call(
        body, name="gdn_scan_bwd", grid=(groups, SCAN_PARTS_BWD),
        in_specs=[mix_s, hs, gz_s, vec, hs, hs, hs, hs, a_s, gl_s, st_s, pl.BlockSpec(memory_space=pl.ANY)] + ex_in_specs,
        out_specs=[gz_s, hs, hs, hs, hs, a_s, gl_s, vec] + ex_out_specs,
        out_shape=[jax.ShapeDtypeStruct((S, DPROJ_PAD), BF16), tok, tok, tok, tok,
                   jax.ShapeDtypeStruct((NGH, S, CHUNK), F32), jax.ShapeDtypeStruct((NGH, NCH, 1, LANES), F32),
                   jax.ShapeDtypeStruct((1, GHD), F32)] + ex_out_shape,
        input_output_aliases={11: 0},
        scratch_shapes=[pltpu.VMEM((S // SCAN_PARTS_BWD, SCAN_HEADS_BWD * GHD), F32),
                        pltpu.VMEM((SCAN_HEADS_BWD, GHD, GHD), F32)] + ex_scratch,
        compiler_params=_cparams(),
    )(dmix, o, proj, w_norm, u, w, qd, kd, a, gl, states, dproj, *ex_in)
    return res[:8], res[8:]


def _place():
    return lax.axis_index("x"), lax.axis_index("y"), lax.axis_index("c")


def _other_chips(x, y):
    return [(1 - x, y), (x, 1 - y), (1 - x, 1 - y)]


HBM = pl.BlockSpec(memory_space=pltpu.HBM)
VMEM = pl.BlockSpec(memory_space=pltpu.VMEM)


def _half_rows(ref_or_rows, half):
    rows = ref_or_rows // 2
    return pl.ds(pl.multiple_of(half * rows, rows), rows)


class _Exchange:
    def __init__(self, inputs, out_shape, n_sems, start, finish=None, middle=None, rest=None):
        self.inputs, self.out_shape, self.n_sems, self.start = inputs, out_shape, n_sems, start
        if finish is None:
            def finish(*refs):
                middle(*refs)
                rest(*refs)
        self.finish = finish
        self.middle = middle if middle is not None else (lambda *refs: None)
        self.rest = rest if rest is not None else finish

    def sem_shapes(self):
        return [pltpu.SemaphoreType.DMA((self.n_sems,)), pltpu.SemaphoreType.DMA((self.n_sems,))]

    def split(self, refs):
        n_in, n_out = len(self.inputs), len(self.out_shape)
        return refs[:n_in], refs[n_in:n_in + n_out], refs[n_in + n_out], refs[n_in + n_out + 1]


def _run_exchange(ex, name):
    def body(*refs):
        parts = ex.split(refs)
        ex.start(*parts)
        ex.finish(*parts)

    return pl.pallas_call(
        body, name=name, in_specs=[HBM] * len(ex.inputs), out_specs=[HBM] * len(ex.out_shape), out_shape=ex.out_shape,
        scratch_shapes=ex.sem_shapes(), compiler_params=_cparams(),
    )(*ex.inputs)


def _allgather_exchange(shards, whole=()):
    n, nw = len(shards), len(whole)
    slots = 8

    def plan(src, outs, send_sems, recv_sems):
        x, y, c = _place()
        via_x, via_y, diagonal = _other_chips(x, y)
        id_x, id_y, id_diagonal = [2 * chip[0] + chip[1] for chip in (via_x, via_y, diagonal)]
        me, sibling = (x, y, c), (x, y, 1 - c)

        def rows_of(a, half, quarter):
            total = src[a].shape[0]
            if quarter is None:
                return _half_rows(total, half)
            return pl.ds(pl.multiple_of(half * (total // 2) + quarter * (total // 4), total // 4), total // 4)

        def copy(a, k, chip_index, half, quarter, to, from_src=False):
            rows = rows_of(a, half, quarter)
            dst = outs[a].at[chip_index, rows]
            return pltpu.make_async_remote_copy(
                src_ref=src[a].at[rows] if from_src else dst, dst_ref=dst, send_sem=send_sems.at[slots * a + k],
                recv_sem=recv_sems.at[slots * a + k], device_id=to, device_id_type=MESH)

        def whole_copy(b, k, chip_index, to):
            return pltpu.make_async_remote_copy(
                src_ref=src[n + b], dst_ref=outs[n + b].at[chip_index], send_sem=send_sems.at[slots * n + 3 * b + k],
                recv_sem=recv_sems.at[slots * n + 3 * b + k], device_id=to, device_id_type=MESH)

        first, stages, last = [], [], []
        for a in range(n):
            first += [copy(a, 0, 2 * x + y, c, None, (*via_x, c), True), copy(a, 1, 2 * x + y, c, None, (*via_y, c), True)]
            stages.append([
                (copy(a, 0, id_x, c, None, me),
                 [copy(a, 2, id_x, c, 0, (*via_y, c)), copy(a, 4, id_x, c, None, sibling)]),
                (copy(a, 1, id_y, c, None, me),
                 [copy(a, 3, id_y, c, 1, (*via_x, c)), copy(a, 5, id_y, c, None, sibling)]),
                (copy(a, 2, id_diagonal, c, 0, me), [copy(a, 6, id_diagonal, c, 0, sibling)]),
                (copy(a, 3, id_diagonal, c, 1, me), [copy(a, 7, id_diagonal, c, 1, sibling)]),
            ])
            last += [copy(a, 4, id_x, 1 - c, None, me), copy(a, 5, id_y, 1 - c, None, me),
                     copy(a, 6, id_diagonal, 1 - c, 0, me), copy(a, 7, id_diagonal, 1 - c, 1, me)]
        for b in range(nw):
            for k, (chip, index) in enumerate(((via_x, id_x), (via_y, id_y), (diagonal, id_diagonal))):
                first.append(whole_copy(b, k, 2 * x + y, (*chip, c)))
                last.append(whole_copy(b, k, index, me))
        return first, stages, last

    def start(*refs):
        for cp in plan(*refs)[0]:
            cp.start()

    def pass_on(stages, which):
        for stage in which:
            for per_shard in stages:
                lands, onward = per_shard[stage]
                lands.wait_recv()
                for cp in onward:
                    cp.start()

    def middle(*refs):
        pass_on(plan(*refs)[1], (0, 1))

    def rest(*refs):
        first, stages, last = plan(*refs)
        pass_on(stages, (2, 3))
        for cp in last:
            cp.wait_recv()
        for cp in first + [cp for per_shard in stages for _, onward in per_shard for cp in onward]:
            cp.wait_send()

    out_shape = [jax.ShapeDtypeStruct((NCHIP,) + s.shape, s.dtype) for s in list(shards) + list(whole)]
    return _Exchange(list(shards) + list(whole), out_shape, slots * n + 3 * nw, start, middle=middle, rest=rest)


def _with_own(gathered, own):
    x, y, _ = _place()
    return lax.dynamic_update_index_in_dim(gathered, own, 2 * x + y, axis=0)


def _simple_exchange(inputs, out_shape, copies_of):
    def start(*refs):
        for cp in copies_of(*refs):
            cp.start()

    def finish(*refs):
        for cp in copies_of(*refs):
            cp.wait()

    return _Exchange(list(inputs), out_shape, len(out_shape) * 3, start, finish)


def _pair_exchange(grads):
    def copies_of(src, outs, send_sems, recv_sems):
        x, y, c = _place()
        return [pltpu.make_async_remote_copy(
            src_ref=src[a].at[:, _half_rows(src[a].shape[1], 1 - c)], dst_ref=outs[a], send_sem=send_sems.at[a],
            recv_sem=recv_sems.at[a], device_id=(x, y, 1 - c), device_id_type=MESH) for a in range(len(src))]

    return _simple_exchange(
        grads, [jax.ShapeDtypeStruct((g.shape[0], g.shape[1] // 2, g.shape[2]), g.dtype) for g in grads], copies_of)


def _pair_sum(grads, theirs, name):
    n = len(grads)

    def body(*refs):
        south = lax.axis_index("c") == 0
        for a in range(n):
            g = refs[a][...]
            half = g.shape[0] // 2
            mine = jnp.where(south, g[:half], g[half:])
            refs[2 * n + a][...] = (mine.astype(F32) + refs[n + a][...].astype(F32)).astype(BF16)

    def specs(arrs):
        return [pl.BlockSpec((None,) + g.shape[1:], lambda j: (j, 0, 0)) for g in arrs]

    return pl.pallas_call(
        body, name=name, grid=(NCHIP,), in_specs=specs(grads) + specs(theirs), out_specs=specs(theirs),
        out_shape=[jax.ShapeDtypeStruct(g.shape, BF16) for g in theirs], compiler_params=_cparams(),
    )(*grads, *theirs)


def _chip_exchange(parts):
    def copies_of(src, outs, send_sems, recv_sems):
        x, y, c = _place()
        return [pltpu.make_async_remote_copy(
            src_ref=src[a].at[2 * chip[0] + chip[1]], dst_ref=outs[a].at[k], send_sem=send_sems.at[3 * a + k],
            recv_sem=recv_sems.at[3 * a + k], device_id=(*chip, c), device_id_type=MESH)
            for a in range(len(src)) for k, chip in enumerate(_other_chips(x, y))]

    return _simple_exchange(parts, [jax.ShapeDtypeStruct((NCHIP - 1,) + p.shape[1:], p.dtype) for p in parts], copies_of)


def _chip_sum(parts, received, exchange=None):
    n = len(parts)
    steps = 4
    ex_in, ex_in_specs, ex_out_specs, ex_out_shape, ex_scratch = _hosted(exchange)

    def body(*refs):
        ex_refs = refs[2 * n:2 * n + len(ex_in)] + refs[3 * n + len(ex_in):]
        if exchange is not None:
            @pl.when(pl.program_id(0) == 0)
            def _():
                exchange.start(*exchange.split(ex_refs))

        chip = 2 * lax.axis_index("x") + lax.axis_index("y")
        for a in range(n):
            p, r = refs[a], refs[n + a]
            own = jnp.where(chip == 0, p[0], jnp.where(chip == 1, p[1], jnp.where(chip == 2, p[2], p[3])))
            refs[2 * n + len(ex_in) + a][...] = ((own.astype(F32) + r[0].astype(F32)) + r[1].astype(F32)) + r[2].astype(F32)

        if exchange is not None:
            @pl.when(pl.program_id(0) == steps - 1)
            def _():
                exchange.finish(*exchange.split(ex_refs))

    def specs(arrs):
        return [pl.BlockSpec((g.shape[0], g.shape[1] // steps, g.shape[2]), lambda i: (0, i, 0)) for g in arrs]

    out_specs = [pl.BlockSpec((g.shape[1] // steps, g.shape[2]), lambda i: (i, 0)) for g in parts]
    res = pl.pallas_call(
        body, name="grads_chip_sum", grid=(steps,), in_specs=specs(parts) + specs(received) + ex_in_specs,
        out_specs=out_specs + ex_out_specs, out_shape=[jax.ShapeDtypeStruct(g.shape[1:], F32) for g in parts] + ex_out_shape,
        scratch_shapes=ex_scratch, compiler_params=_cparams(),
    )(*parts, *received, *ex_in)
    return res[:n], res[n:]


def _pair_share(halves):
    def copies_of(src, outs, send_sems, recv_sems):
        x, y, c = _place()
        return [pltpu.make_async_remote_copy(
            src_ref=src[a], dst_ref=outs[a], send_sem=send_sems.at[a], recv_sem=recv_sems.at[a],
            device_id=(x, y, 1 - c), device_id_type=MESH) for a in range(len(src))]

    return _simple_exchange(halves, [jax.ShapeDtypeStruct(h.shape, F32) for h in halves], copies_of)


def _adamw_math(w, g, m, v):
    nm = ADAM_B1 * m + (1.0 - ADAM_B1) * g
    nv = ADAM_B2 * v + (1.0 - ADAM_B2) * jnp.square(g)
    m_hat = nm / (1.0 - ADAM_B1 ** ADAM_STEP)
    v_hat = nv / (1.0 - ADAM_B2 ** ADAM_STEP)
    return -ADAM_LR * (m_hat / (jnp.sqrt(v_hat) + ADAM_EPS) + ADAM_WD * w), nm, nv


def _adamw_big(ws, g_mine, g_theirs, ms, vs, exchange=None):
    n = len(ws)
    steps = 8
    ex_in, ex_in_specs, ex_out_specs, ex_out_shape, ex_scratch = _hosted(exchange)

    def body(*refs):
        ex_refs = refs[5 * n:5 * n + len(ex_in)] + refs[9 * n + len(ex_in):]
        outs = refs[5 * n + len(ex_in):9 * n + len(ex_in)]
        if exchange is not None:
            @pl.when(pl.program_id(0) == 0)
            def _():
                exchange.start(*exchange.split(ex_refs))

        own_half = (pl.program_id(0) // (steps // 2)) == lax.axis_index("c")
        for a in range(n):
            g = jnp.where(own_half, refs[n + a][...], refs[2 * n + a][...])
            d, nm, nv = _adamw_math(refs[a][...], g, refs[3 * n + a][...], refs[4 * n + a][...])
            outs[a][...] = g
            outs[n + a][...] = d
            outs[2 * n + a][...] = nm
            outs[3 * n + a][...] = nv

        if exchange is not None:
            @pl.when(pl.program_id(0) == steps - 1)
            def _():
                exchange.finish(*exchange.split(ex_refs))

    specs = [pl.BlockSpec((w.shape[0] // steps, w.shape[1]), lambda i: (i, 0)) for w in ws]
    half_specs = [pl.BlockSpec((g.shape[0] // (steps // 2), g.shape[1]), lambda i: (i % (steps // 2), 0)) for g in g_mine]
    shapes = [jax.ShapeDtypeStruct(w.shape, F32) for w in ws]
    res = pl.pallas_call(
        body, name="adamw_big", grid=(steps,), in_specs=specs + half_specs * 2 + specs * 2 + ex_in_specs,
        out_specs=specs * 4 + ex_out_specs, out_shape=shapes * 4 + ex_out_shape, scratch_shapes=ex_scratch,
        compiler_params=_cparams(),
    )(*ws, *g_mine, *g_theirs, *ms, *vs, *ex_in)
    return res[:n], res[n:2 * n], res[2 * n:3 * n], res[3 * n:4 * n], res[4 * n:]


def _adamw_in(w, g_mine, g_theirs, m, v):
    half = D // 2

    def body(w_ref, gm_ref, gt_ref, m_ref, v_ref, g_out, d_out, nm_out, nv_out, g_ref):
        south = lax.axis_index("c") == 0
        g_ref[0:half, :] = jnp.where(south, gm_ref[...], gt_ref[...])
        g_ref[half:D, :] = jnp.where(south, gt_ref[...], gm_ref[...])
        g = g_ref[0:CW, :]
        d, nm, nv = _adamw_math(w_ref[...], g, m_ref[...], v_ref[...])
        g_out[...] = g
        d_out[...] = d
        nm_out[...] = nm
        nv_out[...] = nv

    spec = pl.BlockSpec((CW, LANES), lambda i: (0, i))
    half_spec = pl.BlockSpec((half, LANES), lambda i: (0, i))
    return pl.pallas_call(
        body, name="adamw_in", grid=(D // LANES,), in_specs=[spec, half_spec, half_spec, spec, spec], out_specs=[spec] * 4,
        out_shape=[jax.ShapeDtypeStruct((CW, D), F32)] * 4, scratch_shapes=[pltpu.VMEM((D, LANES), F32)],
        compiler_params=_cparams(),
    )(w, g_mine, g_theirs, m, v)


NORM_NAMES = ("pre_mix_norm", "post_mix_norm", "pre_mlp_norm", "post_mlp_norm")
SMALL_NAMES = NORM_NAMES + ("gdn_conv_w", "fox_f_bias", "gdn_dt_bias", "gdn_a_log", "fox_out_norm", "gdn_out_norm")
CONV_COLS = 3 * DGDN // NCHIP


def _small_gather(d_norms, d_conv, sums, d_fox_norm, d_gdn_norm, loss_row):
    n_arrays = 6
    n_remote = n_arrays * (NDEV - 1)

    def copies_of(src, outs, send_sems, recv_sems):
        x, y, c = _place()
        me = 4 * x + 2 * y + c

        def from_me(chip_index):
            cols = pl.ds(pl.multiple_of(chip_index * CONV_COLS, LANES), CONV_COLS)
            return [src[0], src[1].at[:, cols], src[2], src[3], src[4], src[5]]

        local = [pltpu.make_async_copy(s, outs[a].at[me], send_sems.at[n_remote + a]) for a, s in enumerate(from_me(2 * x + y))]
        remote = []
        for k in range(1, NDEV):
            px, py, pc = x ^ ((k >> 2) & 1), y ^ ((k >> 1) & 1), c ^ (k & 1)
            remote += [pltpu.make_async_remote_copy(
                src_ref=s, dst_ref=outs[a].at[me], send_sem=send_sems.at[n_arrays * (k - 1) + a],
                recv_sem=recv_sems.at[n_arrays * (k - 1) + a], device_id=(px, py, pc), device_id_type=MESH)
                for a, s in enumerate(from_me(2 * px + py))]
        return local + remote

    def start(*refs):
        for cp in copies_of(*refs):
            cp.start()

    def finish(*refs):
        for cp in copies_of(*refs):
            cp.wait()

    shapes = [(4, D), (CONV_K, CONV_COLS), (8, LANES), (1, LANES), (1, LANES), (1, LANES)]
    return _Exchange([d_norms, d_conv, sums, d_fox_norm, d_gdn_norm, loss_row],
                     [jax.ShapeDtypeStruct((NDEV,) + s, F32) for s in shapes], n_remote + n_arrays, start, finish)


def _small_adamw(gathered, ws, ms, vs):
    n = len(SMALL_NAMES)
    ng = len(gathered)

    def body(*refs):
        def total(buf):
            acc = buf[0]
            for i in range(1, NDEV):
                acc = acc + buf[i]
            return acc

        t_norms, t_conv, t_sums, t_fn, t_gn, t_loss = [total(r) for r in refs[:ng]]
        w_refs, m_refs, v_refs = refs[ng:ng + n], refs[ng + n:ng + 2 * n], refs[ng + 2 * n:ng + 3 * n]
        outs = refs[ng + 3 * n:]
        outs[4 * n][...] = t_loss
        grads = [t_norms[i:i + 1, :] for i in range(4)] + [
            t_conv, t_sums[0:1, 0:NFH], t_sums[1:2, 0:NGH], t_sums[2:3, 0:NGH], t_fn[:, 0:FHD], t_gn]
        for a in range(n):
            d, nm, nv = _adamw_math(w_refs[a][...], grads[a], m_refs[a][...], v_refs[a][...])
            outs[a][...] = grads[a]
            outs[n + a][...] = d
            outs[2 * n + a][...] = nm
            outs[3 * n + a][...] = nv

    def whole(arr):
        return pl.BlockSpec(arr.shape, lambda i: (0,) * arr.ndim)

    res = pl.pallas_call(
        body, name="small_adamw", grid=(1,), in_specs=[whole(t) for t in gathered] + [whole(w) for w in ws] * 3,
        out_specs=[whole(w) for w in ws] * 4 + [pl.BlockSpec((1, LANES), lambda i: (0, 0))],
        out_shape=[jax.ShapeDtypeStruct(w.shape, F32) for w in ws] * 4 + [jax.ShapeDtypeStruct((1, LANES), F32)],
        compiler_params=_cparams(),
    )(*gathered, *ws, *ms, *vs)
    return res[:n], res[n:2 * n], res[2 * n:3 * n], res[3 * n:4 * n], res[4 * n]


CW = DPROJ // NCHIP
PROJ_RUNS = tuple((part * DFOX + hp * LANES, part * DFOX + (hp + 1) * LANES, (3 * hp + part) * LANES)
                  for hp in range(NPAIR) for part in range(3)) + (
    (1536, 1544, BLK_SMALL * LANES), (1544, 3080, BLK_GDN * LANES), (3080, 3088, BLK_SMALL * LANES + 8),
    (3088, 3600, BLK_GZ * LANES))


def _proj_pieces():
    pieces = []
    for lo, hi, at in PROJ_RUNS:
        while lo < hi:
            j = lo // CW
            end = min(hi, (j + 1) * CW)
            pieces.append((j, lo - j * CW, at, end - lo))
            at, lo = at + end - lo, end
    return pieces


RT = 256


def _to_padded_rows(gathered):
    def body(src_ref, out_ref, blocks_ref, rows_ref):
        blocks_ref[...] = src_ref[...].astype(F32)
        rows_ref[...] = jnp.zeros_like(rows_ref)
        for j, start, at, n in _proj_pieces():
            rows_ref[at:at + n, :] = blocks_ref[j, start:start + n, :]
        out_ref[...] = rows_ref[...].astype(out_ref.dtype)

    return pl.pallas_call(
        body, name="proj_rows_in", grid=(D // RT,), in_specs=[pl.BlockSpec((NCHIP, D, RT), lambda i: (0, 0, i))],
        out_specs=pl.BlockSpec((DPROJ_PAD, RT), lambda i: (0, i)), out_shape=jax.ShapeDtypeStruct((DPROJ_PAD, D), gathered.dtype),
        scratch_shapes=[pltpu.VMEM((NCHIP, D, RT), F32), pltpu.VMEM((DPROJ_PAD, RT), F32)], compiler_params=_cparams(),
    )(gathered)


def _from_padded_rows(w):
    def body(src_ref, out_ref, rows_ref, blocks_ref):
        rows_ref[...] = src_ref[...].astype(F32)
        blocks_ref[...] = jnp.zeros_like(blocks_ref)
        for j, start, at, n in _proj_pieces():
            blocks_ref[j, start:start + n, :] = rows_ref[at:at + n, :]
        out_ref[...] = blocks_ref[...].astype(out_ref.dtype)

    return pl.pallas_call(
        body, name="proj_rows_out", grid=(D // RT,), in_specs=[pl.BlockSpec((DPROJ_PAD, RT), lambda i: (0, i))],
        out_specs=pl.BlockSpec((NCHIP, D, RT), lambda i: (0, 0, i)), out_shape=jax.ShapeDtypeStruct((NCHIP, D, D), w.dtype),
        scratch_shapes=[pltpu.VMEM((DPROJ_PAD, RT), F32), pltpu.VMEM((NCHIP, D, RT), F32)], compiler_params=_cparams(),
    )(w)


def _local_step(x, target, first_weights, late_weights, reduce_late, reduce_in, pre_mix_norm, fox_f_bias, fox_out_norm,
                gdn_a_log, gdn_dt_bias, gdn_out_norm, post_mix_norm, pre_mlp_norm, post_mlp_norm):
    bias_vec = jnp.zeros((1, LANES), F32).at[0, 0:NFH].set(fox_f_bias).at[0, LANE_G:LANE_G + NGH].set(gdn_dt_bias)
    alog_vec = jnp.zeros((1, LANES), F32).at[0, LANE_G:LANE_G + NGH].set(gdn_a_log)
    w2 = jnp.concatenate([fox_out_norm, fox_out_norm], axis=1)

    h, first = _pre_norm(x, pre_mix_norm, exchange=first_weights[0])
    win_p, conv_w = first_weights[1](first)
    proj = _matmul(h, win_p, tb=True, tm=2048, tn=768, tk=1024, name="mm_proj", exchange=late_weights[0])
    proj, late_a = proj if late_weights[0] is not None else (proj, [])
    gates = _gates(proj, bias_vec, alog_vec)
    mix, fox_o, lse, late_b = _fox_fwd(proj, gates, w2, exchange=late_weights[1])
    qkv = _gdn_pre(proj, conv_w)
    (u, w, qd, kd, a_intra, gl, t_inv), _ = _gdn_prep(qkv, gates)
    wout, wup3 = late_weights[3](late_a, late_b)
    mix, gdn_raw, states = _gdn_scan(u, w, qd, kd, a_intra, gl, proj, gdn_out_norm, mix)
    mixed = _matmul(mix, wout, tm=2048, tk=1024, name="mm_out")
    x1, h2 = _post_mix(x, mixed, post_mix_norm, pre_mlp_norm)

    def relu2(acc):
        r = jnp.maximum(acc, 0.0)
        return r, r * r

    up_act = _matmul(h2, wup3, b3=True, tm=1024, tn=1024, tk=1024, out_dtypes=(BF16, BF16), epilogue=relu2,
                     name="mm_up", exchange=late_weights[2])
    (up_relu, act), late_c = up_act if late_weights[2] is not None else (up_act, [])
    wdown = late_weights[4](late_c)
    y = _matmul(act, wdown, tm=1024, tk=DFF, name="mm_down")
    dx2, dy, d_post_mlp, loss_row = _loss_head(x1, y, post_mlp_norm, target)

    dwdown = _matmul(act, dy, ta=True, tm=1024, tn=1024, tk=2048, out_dtypes=(BF16,), name="mm_dwdown")

    def relu2_bwd(acc, r):
        return (acc * 2.0 * r.astype(F32),)

    dup = _matmul(dy, wdown, tb=True, tm=1024, tn=1024, tk=1024, out_dtypes=(BF16,), extra=(up_relu,), epilogue=relu2_bwd,
                  name="mm_dact")
    dwup3 = _matmul(h2, dup, ta=True, tm=1024, tn=1024, tk=2048, out_dtypes=(BF16,), o3=True, name="mm_dwup")
    dh2 = _matmul(dup, wup3, tb=True, b3=True, tm=1024, tk=DFF, name="mm_dh2")
    dx1, dmixed, d_pre_mlp, d_post_mix = _mid_bwd(dh2, x1, pre_mlp_norm, dx2, mixed, post_mix_norm)
    dwout = _matmul(mix, dmixed, ta=True, tm=1024, tn=1024, tk=2048, out_dtypes=(BF16,), name="mm_dwout")
    dmix = _matmul(dmixed, wout, tb=True, tm=2048, tk=1024, name="mm_dmix")

    dfox, delta, d_fox_norm, from_sibling = _fox_norm_bwd(fox_o, dmix, w2, exchange=reduce_late[0](dwout, dwup3, dwdown))
    dproj, dcum_fox, reduced_a = _fox_bwd(proj, dfox, gates, lse, delta, exchange=reduce_late[1](from_sibling))
    (dproj, du, dw, dqd, dkd, da, dgl, d_gdn_norm), reduced_b = _gdn_scan_bwd(
        dmix, gdn_raw, proj, gdn_out_norm, u, w, qd, kd, a_intra, gl, states, dproj, exchange=reduce_late[2]())
    dqkv, dgates_gdn, reduced_c = _gdn_prep_bwd(qkv, gates, t_inv, du, dw, dqd, dkd, da, dgl, exchange=reduce_late[3]())
    reduced_late = (reduced_a, reduced_b, reduced_c)
    dproj, d_conv = _gdn_pre_bwd(proj, conv_w, dqkv, dproj)
    dproj, sums = _gates_bwd(proj, bias_vec, alog_vec, dgates_gdn, dcum_fox, dproj)

    dwin_p = _matmul(dproj, h, ta=True, tm=1280, tn=1024, tk=2048, out_dtypes=(BF16,), name="mm_dwin")
    exchange_in = reduce_in(dwin_p)
    dh = _matmul(dproj, win_p, tm=1024, tk=DPROJ_PAD, name="mm_dh", exchange=exchange_in)
    dh, reduced_in = dh if exchange_in is not None else (dh, [])
    grad_x, d_pre_mix = _pre_norm_bwd(dh, x, pre_mix_norm, dx1)

    d_norms = jnp.concatenate([d_pre_mix, d_post_mix, d_pre_mlp, d_post_mlp], axis=0)
    return grad_x, (d_norms, d_conv, sums, d_fox_norm, d_gdn_norm, loss_row), reduced_late, reduced_in


def kernel(x, pre_mix_norm, w_in, fox_f_bias, fox_out_norm, gdn_conv_w, gdn_a_log, gdn_dt_bias, gdn_out_norm, w_out, post_mix_norm, pre_mlp_norm, w_up, w_down, post_mlp_norm, loss_target, m_pre_mix_norm, m_w_in, m_fox_f_bias, m_fox_out_norm, m_gdn_conv_w, m_gdn_a_log, m_gdn_dt_bias, m_gdn_out_norm, m_w_out, m_post_mix_norm, m_pre_mlp_norm, m_w_up, m_w_down, m_post_mlp_norm, v_pre_mix_norm, v_w_in, v_fox_f_bias, v_fox_out_norm, v_gdn_conv_w, v_gdn_a_log, v_gdn_dt_bias, v_gdn_out_norm, v_w_out, v_post_mix_norm, v_pre_mlp_norm, v_w_up, v_w_down, v_post_mlp_norm):
    weights = dict(pre_mix_norm=pre_mix_norm, w_in=w_in, fox_f_bias=fox_f_bias, fox_out_norm=fox_out_norm, gdn_conv_w=gdn_conv_w,
                   gdn_a_log=gdn_a_log, gdn_dt_bias=gdn_dt_bias, gdn_out_norm=gdn_out_norm, w_out=w_out, post_mix_norm=post_mix_norm,
                   pre_mlp_norm=pre_mlp_norm, w_up=w_up, w_down=w_down, post_mlp_norm=post_mlp_norm)
    m_in = dict(pre_mix_norm=m_pre_mix_norm, w_in=m_w_in, fox_f_bias=m_fox_f_bias, fox_out_norm=m_fox_out_norm, gdn_conv_w=m_gdn_conv_w,
                gdn_a_log=m_gdn_a_log, gdn_dt_bias=m_gdn_dt_bias, gdn_out_norm=m_gdn_out_norm, w_out=m_w_out, post_mix_norm=m_post_mix_norm,
                pre_mlp_norm=m_pre_mlp_norm, w_up=m_w_up, w_down=m_w_down, post_mlp_norm=m_post_mlp_norm)
    v_in = dict(pre_mix_norm=v_pre_mix_norm, w_in=v_w_in, fox_f_bias=v_fox_f_bias, fox_out_norm=v_fox_out_norm, gdn_conv_w=v_gdn_conv_w,
                gdn_a_log=v_gdn_a_log, gdn_dt_bias=v_gdn_dt_bias, gdn_out_norm=v_gdn_out_norm, w_out=v_w_out, post_mix_norm=v_post_mix_norm,
                pre_mlp_norm=v_pre_mlp_norm, w_up=v_w_up, w_down=v_w_down, post_mlp_norm=v_post_mlp_norm)
    order_w = ("pre_mix_norm", "w_in", "fox_f_bias", "fox_out_norm", "gdn_conv_w", "gdn_a_log", "gdn_dt_bias", "gdn_out_norm", "w_out",
               "post_mix_norm", "pre_mlp_norm", "w_up", "w_down", "post_mlp_norm")
    big = ("w_in", "w_out", "w_up", "w_down")

    def row(v):
        return v if v.ndim == 2 else v.reshape(1, -1)

    win_shard = jnp.pad(w_in.T.astype(BF16), ((0, D - CW), (0, 0)))

    def resolve_first(gathered):
        win_g, conv_g = gathered
        return (_to_padded_rows(_with_own(win_g, win_shard)),
                _with_own(conv_g, gdn_conv_w).transpose(1, 0, 2).reshape(CONV_K, 3 * DGDN))

    late_shards = [weights[n].astype(BF16) for n in big[1:]]

    def resolve_out_up(gathered_out, gathered_up):
        return _with_own(gathered_out[0], late_shards[0]).reshape(D, D), _with_own(gathered_up[0], late_shards[1])

    def resolve_down(gathered_down):
        return _with_own(gathered_down[0], late_shards[2]).reshape(DFF, D)

    pair_sums, late_blocks = {}, []

    def pair_summed(names, blocks, theirs):
        for n, s in zip(names, _pair_sum(blocks, theirs, "grads_pair_sum_" + names[0])):
            pair_sums[n] = s

    def late_pair_exchange(dwout, dwup3, dwdown):
        late_blocks.extend([dwout.reshape(NCHIP, D // NCHIP, D), dwup3, dwdown.reshape(NCHIP, DFF // NCHIP, D)])
        return _pair_exchange(late_blocks)

    def late_chip_exchange(theirs):
        pair_summed(big[1:], late_blocks, theirs)
        return _chip_exchange([pair_sums["w_up"], pair_sums["w_down"]])

    def reduce_in(dwin_p):
        blocks = [_from_padded_rows(dwin_p)]
        pair_summed(big[:1], blocks, _run_exchange(_pair_exchange(blocks), "grads_pair_exchange_w_in"))
        return _chip_exchange([pair_sums["w_in"]])

    grad_x, small, received_late, received_in = _local_step(
        x[0], loss_target[0], (_allgather_exchange([win_shard], whole=[gdn_conv_w]), resolve_first),
        tuple(_allgather_exchange([shard]) for shard in late_shards) + (resolve_out_up, resolve_down),
        (late_pair_exchange, late_chip_exchange, lambda: None, lambda: _chip_exchange([pair_sums["w_out"]])),
        reduce_in, row(pre_mix_norm), fox_f_bias, row(fox_out_norm), gdn_a_log, gdn_dt_bias,
        row(gdn_out_norm), row(post_mix_norm), row(pre_mlp_norm), row(post_mlp_norm))
    received_mlp, _, received_out = received_late

    g_mine, small_gathered = _chip_sum(
        [pair_sums[n] for n in big], list(received_in[:1]) + list(received_out[:1]) + list(received_mlp[:2]),
        exchange=_small_gather(*small))
    g_theirs = _run_exchange(_pair_share(g_mine), "grads_pair_share")

    g_big, d_big, nm_big, nv_big, _ = _adamw_big(
        [weights[n] for n in big[1:]], g_mine[1:], g_theirs[1:], [m_in[n] for n in big[1:]], [v_in[n] for n in big[1:]])
    in_t = _adamw_in(w_in.T, g_mine[0], g_theirs[0], m_w_in.T, v_w_in.T)
    g_small, d_small, nm_small, nv_small, loss_total = _small_adamw(
        small_gathered, [row(weights[n]) for n in SMALL_NAMES], [row(m_in[n]) for n in SMALL_NAMES],
        [row(v_in[n]) for n in SMALL_NAMES])

    grads, delta, new_m, new_v = {}, {}, {}, {}
    grads["w_in"], delta["w_in"], new_m["w_in"], new_v["w_in"] = [t.T for t in in_t]
    for i, n in enumerate(big[1:]):
        grads[n], delta[n], new_m[n], new_v[n] = g_big[i], d_big[i], nm_big[i], nv_big[i]
    for i, n in enumerate(SMALL_NAMES):
        shape = weights[n].shape
        grads[n], delta[n], new_m[n], new_v[n] = (g_small[i].reshape(shape), d_small[i].reshape(shape),
                                                  nm_small[i].reshape(shape), nv_small[i].reshape(shape))
    return (loss_total[0, 0], grad_x[None], *[grads[n] for n in order_w], *[delta[n] for n in order_w], *[new_m[n] for n in order_w],
            *[new_v[n] for n in order_w])
```

```python
import jax
import jax.numpy as jnp
from jax import lax
from jax.experimental import pallas as pl
from jax.experimental.pallas import tpu as pltpu

F32 = jnp.float32
BF16 = jnp.bfloat16
MESH = pl.DeviceIdType.MESH

S = 2048
D = 1024
NFH, FHD = 8, 64
NPAIR = NFH // 2
NGH, GHD = 4, 128
DFOX = NFH * FHD
DGDN = NGH * GHD
CHUNK = 64
NCH = S // CHUNK
CONV_K = 4
DFF = 4 * D
EPS = 1e-6
DPROJ = 3600
LANES = 128
DPROJ_PAD = 3840
BLK_GDN = 12
BLK_GZ = 24
BLK_SMALL = 28
NCHIP = 4
NDEV = 8
VMEM_LIMIT = 56 * 1024 * 1024

ADAM_LR = 0.001
ADAM_B1 = 0.9
ADAM_B2 = 0.999
ADAM_EPS = 1e-08
ADAM_WD = 0.01
ADAM_STEP = 10


def _cparams(**kw):
    return pltpu.CompilerParams(vmem_limit_bytes=VMEM_LIMIT, **kw)


def _dn(ca, cb):
    return (((ca,), (cb,)), ((), ()))


def _dot(a, b, ca=1, cb=0):
    return lax.dot_general(a.astype(BF16), b.astype(BF16), _dn(ca, cb), preferred_element_type=F32)


def _hdot(a, b, ca=1, cb=0):
    return lax.dot_general(a.astype(F32), b.astype(F32), _dn(ca, cb), precision=lax.Precision.HIGHEST,
                           preferred_element_type=F32)


def _dot3(a, b, ca=1, cb=0):
    a_hi, b_hi = a.astype(BF16), b.astype(BF16)
    a_lo, b_lo = (a - a_hi.astype(F32)).astype(BF16), (b - b_hi.astype(F32)).astype(BF16)
    dn = _dn(ca, cb)
    return (lax.dot_general(a_hi, b_hi, dn, preferred_element_type=F32)
            + (lax.dot_general(a_hi, b_lo, dn, preferred_element_type=F32)
               + lax.dot_general(a_lo, b_hi, dn, preferred_element_type=F32)))


@jax.custom_vjp
def _mm_nn(a, b):
    return _dot(a, b, 1, 0)


def _mm_nn_fwd(a, b):
    return _dot(a, b, 1, 0), (a, b)


def _mm_nn_bwd(res, g):
    a, b = res
    return _dot(g, b, 1, 1), _dot(a, g, 0, 0)


_mm_nn.defvjp(_mm_nn_fwd, _mm_nn_bwd)


@jax.custom_vjp
def _mm_nt(a, b):
    return _dot(a, b, 1, 1)


def _mm_nt_fwd(a, b):
    return _dot(a, b, 1, 1), (a, b)


def _mm_nt_bwd(res, g):
    a, b = res
    return _dot(g, b, 1, 0), _dot(g, a, 0, 0)


_mm_nt.defvjp(_mm_nt_fwd, _mm_nt_bwd)


@jax.custom_vjp
def _saved_inverse(m, t_inv):
    del m
    return t_inv


def _saved_inverse_fwd(m, t_inv):
    del m
    return t_inv, t_inv


def _saved_inverse_bwd(t_inv, g):
    return -_dot3(_dot3(t_inv, g, 0, 0), t_inv, 1, 1), jnp.zeros_like(t_inv)


_saved_inverse.defvjp(_saved_inverse_fwd, _saved_inverse_bwd)


def _sigmoid(z):
    return 1.0 / (1.0 + jnp.exp(-z))


def _softplus(z):
    return jnp.maximum(z, 0.0) + jnp.log(1.0 + jnp.exp(-jnp.abs(z)))


def _silu(z):
    return z * _sigmoid(z)


def _rms_scale(x):
    return lax.rsqrt(jnp.mean(x * x, axis=-1, keepdims=True) + EPS)


def _rms_bwd(x, w, g):
    r = _rms_scale(x)
    gw = g * w
    dx = r * gw - x * (r * r * r) * jnp.mean(gw * x, axis=-1, keepdims=True)
    return dx, g * x * r


def _matmul(a, b, *, name, ta=False, tb=False, tm=512, tn=512, tk=512, out_dtypes=(F32,), b3=False, o3=False,
            extra=(), epilogue=None, exchange=None):
    m, k = (a.shape[1], a.shape[0]) if ta else a.shape
    if b3:
        n = b.shape[1] if tb else b.shape[0] * b.shape[2]
        kb = b.shape[0] * b.shape[2] if tb else b.shape[1]
    else:
        n, kb = (b.shape[0], b.shape[1]) if tb else (b.shape[1], b.shape[0])
    assert kb == k, (name, kb, k)
    tm, tn, tk = min(tm, m), min(tn, n), min(tk, k)
    assert m % tm == 0 and n % tn == 0 and k % tk == 0, (name, m, n, k, tm, tn, tk)
    nk = k // tk
    whole_k_blocks = b3 and tb and not ta and nk == 1 and b.shape[0] > 1
    n_extra = len(extra)
    n_out = len(out_dtypes)
    grid = (m // tm, n // tn, nk)
    ex_in, ex_in_specs, ex_out_specs, ex_out_shape, ex_scratch = _hosted(exchange)

    def body(*refs):
        a_ref, b_ref = refs[0], refs[1]
        extra_refs = refs[2:2 + n_extra]
        first_out = 2 + n_extra + len(ex_in)
        out_refs = refs[first_out:first_out + n_out]
        ex_refs = refs[2 + n_extra:first_out] + refs[first_out + n_out:first_out + n_out + len(ex_out_shape)] + refs[-2:]
        step = [pl.program_id(d) for d in range(3)]

        if exchange is not None:
            @pl.when((step[0] == 0) & (step[1] == 0) & (step[2] == 0))
            def _():
                exchange.start(*exchange.split(ex_refs))

        def finish(acc):
            outs = (acc,) if epilogue is None else epilogue(acc, *[r[...] for r in extra_refs])
            for o_ref, val in zip(out_refs, outs):
                o_ref[...] = val.astype(o_ref.dtype)

        if whole_k_blocks:
            width = b.shape[2]
            part = _dot(a_ref[:, 0:width], b_ref[0], 1, 1)
            for blk in range(1, b.shape[0]):
                part = part + _dot(a_ref[:, blk * width:(blk + 1) * width], b_ref[blk], 1, 1)
        else:
            part = _dot(a_ref[...], b_ref[...], 0 if ta else 1, 1 if tb else 0)
        if nk == 1:
            finish(part)
        else:
            acc_ref = refs[first_out + n_out + len(ex_out_shape)]

            @pl.when(step[2] == 0)
            def _():
                acc_ref[...] = part

            @pl.when(step[2] > 0)
            def _():
                acc_ref[...] += part

            @pl.when(step[2] == nk - 1)
            def _():
                finish(acc_ref[...])

        if exchange is not None:
            flat = (step[0] * grid[1] + step[1]) * nk + step[2]
            total = grid[0] * grid[1] * nk

            @pl.when(flat == total // 2)
            def _():
                exchange.middle(*exchange.split(ex_refs))

            @pl.when(flat == total - 1)
            def _():
                exchange.rest(*exchange.split(ex_refs))

    a_spec = pl.BlockSpec((tk, tm), lambda i, j, kk: (kk, i)) if ta else pl.BlockSpec((tm, tk), lambda i, j, kk: (i, kk))
    if whole_k_blocks:
        b_spec = pl.BlockSpec((b.shape[0], tn, b.shape[2]), lambda i, j, kk: (0, j, 0))
    elif b3 and tb:
        assert b.shape[2] == tk
        b_spec = pl.BlockSpec((None, tn, tk), lambda i, j, kk: (kk, j, 0))
    elif b3:
        assert b.shape[2] == tn
        b_spec = pl.BlockSpec((None, tk, tn), lambda i, j, kk: (j, kk, 0))
    elif tb:
        b_spec = pl.BlockSpec((tn, tk), lambda i, j, kk: (j, kk))
    else:
        b_spec = pl.BlockSpec((tk, tn), lambda i, j, kk: (kk, j))
    tile = pl.BlockSpec((tm, tn), lambda i, j, kk: (i, j))
    out_specs = [tile] * n_out
    out_shape = [jax.ShapeDtypeStruct((m, n), dt) for dt in out_dtypes]
    if o3:
        out_specs[0] = pl.BlockSpec((None, tm, tn), lambda i, j, kk: (j, i, 0))
        out_shape[0] = jax.ShapeDtypeStruct((n // tn, m, tn), out_dtypes[0])
    res = pl.pallas_call(
        body, name=name, grid=grid,
        in_specs=[a_spec, b_spec] + [tile] * n_extra + ex_in_specs, out_specs=out_specs + ex_out_specs,
        out_shape=out_shape + ex_out_shape,
        scratch_shapes=([pltpu.VMEM((tm, tn), F32)] if nk > 1 else []) + ex_scratch,
        compiler_params=_cparams(),
    )(a, b, *extra, *ex_in)
    if exchange is not None:
        return (res[0] if n_out == 1 else res[:n_out]), res[n_out:]
    return res[0] if n_out == 1 else res


TR = 256


def _row_spec(cols):
    return pl.BlockSpec((TR, cols), lambda i: (i, 0))


def _vec_spec(cols):
    return pl.BlockSpec((1, cols), lambda i: (0, 0))


def _pre_norm(x, w, exchange=None):
    ex_in, ex_in_specs, ex_out_specs, ex_out_shape, ex_scratch = _hosted(exchange)

    def body(*refs):
        x_ref, w_ref, h_ref = refs[0], refs[1], refs[2 + len(ex_in)]
        ex_refs = refs[2:2 + len(ex_in)] + refs[3 + len(ex_in):]
        if exchange is not None:
            @pl.when(pl.program_id(0) == 0)
            def _():
                exchange.start(*exchange.split(ex_refs))

        xv = x_ref[...]
        h_ref[...] = (xv * _rms_scale(xv) * w_ref[...]).astype(BF16)

        if exchange is not None:
            @pl.when(pl.program_id(0) == S // TR - 1)
            def _():
                exchange.finish(*exchange.split(ex_refs))

    res = pl.pallas_call(
        body, name="pre_norm", grid=(S // TR,), in_specs=[_row_spec(D), _vec_spec(D)] + ex_in_specs,
        out_specs=[_row_spec(D)] + ex_out_specs, out_shape=[jax.ShapeDtypeStruct((S, D), BF16)] + ex_out_shape,
        scratch_shapes=ex_scratch, compiler_params=_cparams(),
    )(x, w, *ex_in)
    return res[0], res[1:]


def _post_mix(x, mixed, w_post, w_pre_mlp):
    def body(x_ref, m_ref, wp_ref, wm_ref, x1_ref, h2_ref):
        mv = m_ref[...]
        x1 = x_ref[...] + mv * _rms_scale(mv) * wp_ref[...]
        x1_ref[...] = x1
        h2_ref[...] = (x1 * _rms_scale(x1) * wm_ref[...]).astype(BF16)

    return pl.pallas_call(
        body, name="post_mix", grid=(S // TR,),
        in_specs=[_row_spec(D), _row_spec(D), _vec_spec(D), _vec_spec(D)], out_specs=[_row_spec(D), _row_spec(D)],
        out_shape=[jax.ShapeDtypeStruct((S, D), F32), jax.ShapeDtypeStruct((S, D), BF16)], compiler_params=_cparams(),
    )(x, mixed, w_post, w_pre_mlp)


def _loss_head(x1, y, w_post_mlp, target):
    def body(x1_ref, y_ref, w_ref, t_ref, dx2_ref, dy_ref, dw_ref, loss_ref):
        i = pl.program_id(0)
        yv = y_ref[...]
        w = w_ref[...]
        x2 = x1_ref[...] + yv * _rms_scale(yv) * w
        err = x2 - t_ref[...]
        dx2 = err * (1.0 / D)
        dx2_ref[...] = dx2
        dy, dwt = _rms_bwd(yv, w, dx2)
        dy_ref[...] = dy.astype(BF16)

        @pl.when(i == 0)
        def _():
            dw_ref[...] = jnp.zeros_like(dw_ref)
            loss_ref[...] = jnp.zeros_like(loss_ref)

        dw_ref[...] += jnp.sum(dwt, axis=0, keepdims=True)
        part = 0.5 * jnp.sum(jnp.mean(err * err, axis=-1, keepdims=True), axis=0, keepdims=True)
        loss_ref[...] += jnp.broadcast_to(part, loss_ref.shape)

    return pl.pallas_call(
        body, name="loss_head", grid=(S // TR,),
        in_specs=[_row_spec(D), _row_spec(D), _vec_spec(D), _row_spec(D)],
        out_specs=[_row_spec(D), _row_spec(D), _vec_spec(D), _vec_spec(LANES)],
        out_shape=[jax.ShapeDtypeStruct((S, D), F32), jax.ShapeDtypeStruct((S, D), BF16),
                   jax.ShapeDtypeStruct((1, D), F32), jax.ShapeDtypeStruct((1, LANES), F32)],
        compiler_params=_cparams(),
    )(x1, y, w_post_mlp, target)


def _mid_bwd(dh2, x1, w_pre_mlp, dx2, mixed, w_post):
    def body(dh2_ref, x1_ref, wm_ref, dx2_ref, m_ref, wp_ref, dx1_ref, dm_ref, dwm_ref, dwp_ref):
        i = pl.program_id(0)
        dxa, dwm = _rms_bwd(x1_ref[...], wm_ref[...], dh2_ref[...])
        dx1 = dx2_ref[...] + dxa
        dx1_ref[...] = dx1
        dm, dwp = _rms_bwd(m_ref[...], wp_ref[...], dx1)
        dm_ref[...] = dm.astype(BF16)

        @pl.when(i == 0)
        def _():
            dwm_ref[...] = jnp.zeros_like(dwm_ref)
            dwp_ref[...] = jnp.zeros_like(dwp_ref)

        dwm_ref[...] += jnp.sum(dwm, axis=0, keepdims=True)
        dwp_ref[...] += jnp.sum(dwp, axis=0, keepdims=True)

    return pl.pallas_call(
        body, name="mid_bwd", grid=(S // TR,),
        in_specs=[_row_spec(D), _row_spec(D), _vec_spec(D), _row_spec(D), _row_spec(D), _vec_spec(D)],
        out_specs=[_row_spec(D), _row_spec(D), _vec_spec(D), _vec_spec(D)],
        out_shape=[jax.ShapeDtypeStruct((S, D), F32), jax.ShapeDtypeStruct((S, D), BF16),
                   jax.ShapeDtypeStruct((1, D), F32), jax.ShapeDtypeStruct((1, D), F32)],
        compiler_params=_cparams(),
    )(dh2, x1, w_pre_mlp, dx2, mixed, w_post)


def _pre_norm_bwd(dh, x, w, dx1):
    def body(dh_ref, x_ref, w_ref, dx1_ref, dx_ref, dw_ref):
        i = pl.program_id(0)
        dxa, dwt = _rms_bwd(x_ref[...], w_ref[...], dh_ref[...])
        dx_ref[...] = dx1_ref[...] + dxa

        @pl.when(i == 0)
        def _():
            dw_ref[...] = jnp.zeros_like(dw_ref)

        dw_ref[...] += jnp.sum(dwt, axis=0, keepdims=True)

    return pl.pallas_call(
        body, name="pre_norm_bwd", grid=(S // TR,),
        in_specs=[_row_spec(D), _row_spec(D), _vec_spec(D), _row_spec(D)], out_specs=[_row_spec(D), _vec_spec(D)],
        out_shape=[jax.ShapeDtypeStruct((S, D), F32), jax.ShapeDtypeStruct((1, D), F32)], compiler_params=_cparams(),
    )(dh, x, w, dx1)


BQ = 512
NQ = S // BQ
LANE_BETA, LANE_G = 8, 12


def _gate_lanes(shape):
    lane = lax.broadcasted_iota(jnp.int32, shape, 1)
    return lane < LANE_BETA, (lane >= LANE_BETA) & (lane < LANE_G), (lane >= LANE_G) & (lane < LANE_G + NGH)


def _gates(proj, bias_vec, alog_vec):
    def body(s_ref, b_ref, a_ref, o_ref, carry_ref):
        i = pl.program_id(0)

        @pl.when(i == 0)
        def _():
            carry_ref[...] = jnp.zeros_like(carry_ref)

        z = s_ref[...] + b_ref[...]
        tail = jnp.log(1.0 + jnp.exp(-jnp.abs(z)))
        sp = jnp.maximum(z, 0.0) + tail
        lf = jnp.minimum(z, 0.0) - tail
        r = lax.broadcasted_iota(jnp.int32, (BQ, BQ), 0)
        c = lax.broadcasted_iota(jnp.int32, (BQ, BQ), 1)
        tri = (c <= r).astype(F32)
        cum = _hdot(tri, lf) + carry_ref[...]
        carry_ref[...] = cum[BQ - 1:BQ, :]
        is_fox, is_beta, is_g = _gate_lanes(z.shape)
        o_ref[...] = jnp.where(is_fox, cum, jnp.where(is_beta, _sigmoid(z), jnp.where(is_g, -jnp.exp(a_ref[...]) * sp, 0.0)))

    return pl.pallas_call(
        body, name="gates", grid=(NQ,),
        in_specs=[pl.BlockSpec((BQ, LANES), lambda i: (i, BLK_SMALL)), _vec_spec(LANES), _vec_spec(LANES)],
        out_specs=pl.BlockSpec((BQ, LANES), lambda i: (i, 0)), out_shape=jax.ShapeDtypeStruct((S, LANES), F32),
        scratch_shapes=[pltpu.VMEM((1, LANES), F32)], compiler_params=_cparams(),
    )(proj, bias_vec, alog_vec)


def _gates_bwd(proj, bias_vec, alog_vec, dgates_gdn, dcum_fox, dproj):
    def body(s_ref, b_ref, a_ref, dg_ref, dc_ref, dproj_in, dproj_ref, red_ref, carry_ref):
        del dproj_in
        i = pl.program_id(0)

        @pl.when(i == 0)
        def _():
            carry_ref[...] = jnp.zeros_like(carry_ref)
            red_ref[...] = jnp.zeros_like(red_ref)

        z = s_ref[...] + b_ref[...]
        dg = dg_ref[...] + dc_ref[...]
        r = lax.broadcasted_iota(jnp.int32, (BQ, BQ), 0)
        c = lax.broadcasted_iota(jnp.int32, (BQ, BQ), 1)
        upper = (c >= r).astype(F32)
        dlf = _hdot(upper, dg) + carry_ref[...]
        carry_ref[...] = dlf[0:1, :]
        sig = _sigmoid(z)
        g_scale = -jnp.exp(a_ref[...])
        is_fox, is_beta, is_g = _gate_lanes(z.shape)
        ds = jnp.where(is_fox, dlf * (1.0 - sig), jnp.where(is_beta, dg * sig * (1.0 - sig), jnp.where(is_g, dg * g_scale * sig, 0.0)))
        dproj_ref[:, 0:LANES] = ds.astype(BF16)
        dproj_ref[:, LANES:2 * LANES] = jnp.zeros((BQ, LANES), BF16)
        dalog = jnp.where(is_g, dg * g_scale * _softplus(z), 0.0)
        sums = jnp.sum(ds, axis=0, keepdims=True)
        red_ref[0:1, :] += jnp.where(is_fox[0:1], sums, 0.0)
        red_ref[1:2, :] += pltpu.roll(jnp.where(is_g[0:1], sums, 0.0), LANES - LANE_G, 1)
        red_ref[2:3, :] += pltpu.roll(jnp.sum(dalog, axis=0, keepdims=True), LANES - LANE_G, 1)

    blk = pl.BlockSpec((BQ, LANES), lambda i: (NQ - 1 - i, 0))
    return pl.pallas_call(
        body, name="gates_bwd", grid=(NQ,),
        in_specs=[pl.BlockSpec((BQ, LANES), lambda i: (NQ - 1 - i, BLK_SMALL)), _vec_spec(LANES), _vec_spec(LANES), blk, blk,
                  pl.BlockSpec(memory_space=pl.ANY)],
        out_specs=[pl.BlockSpec((BQ, 2 * LANES), lambda i: (NQ - 1 - i, BLK_SMALL // 2)), pl.BlockSpec((8, LANES), lambda i: (0, 0))],
        out_shape=[jax.ShapeDtypeStruct((S, DPROJ_PAD), BF16), jax.ShapeDtypeStruct((8, LANES), F32)],
        input_output_aliases={5: 0},
        scratch_shapes=[pltpu.VMEM((1, LANES), F32)], compiler_params=_cparams(),
    )(proj, bias_vec, alog_vec, dgates_gdn, dcum_fox, dproj)


FOX_SCALE = FHD ** -0.5
FOX_PAIRS = 2
FOX_PAIRS_BWD = 2


def _head_mask(e):
    lane = lax.broadcasted_iota(jnp.int32, (1, LANES), 1)
    return (lane >= e * FHD) & (lane < (e + 1) * FHD)


def _lane_col(vals, index):
    lane = lax.broadcasted_iota(jnp.int32, vals.shape, 1)
    return jnp.sum(jnp.where(lane == index, vals, 0.0), axis=1, keepdims=True)


def _sublane_row(vals, index):
    row = lax.broadcasted_iota(jnp.int32, vals.shape, 0)
    return jnp.sum(jnp.where(row == index, vals, 0.0), axis=0, keepdims=True)


def _pair_cols(c0, c1):
    lane = lax.broadcasted_iota(jnp.int32, (c0.shape[0], 2), 1)
    return jnp.where(lane == 0, c0, c1)


def _split3(x):
    hi = x.astype(BF16).astype(F32)
    rest = x - hi
    mid = rest.astype(BF16).astype(F32)
    return hi, mid, (rest - mid).astype(BF16).astype(F32)


def _fox_operand(vals, e, cum, is_query):
    lane = lax.broadcasted_iota(jnp.int32, (1, LANES), 1)
    base = (1 - e) * FHD
    parts = _split3(cum)
    own = jnp.where(_head_mask(e), vals * FOX_SCALE if is_query else vals, 0.0)
    cum_at, ones_at = (base, base + 3) if is_query else (base + 3, base)
    sign = 1.0 if is_query else -1.0
    out = own + jnp.where((lane >= ones_at) & (lane < ones_at + 3), 1.0, 0.0)
    for i, part in enumerate(parts):
        out = out + jnp.where(lane == cum_at + i, sign * part, 0.0)
    return out.astype(BF16)


def _causal_block():
    return lax.broadcasted_iota(jnp.int32, (BQ, BQ), 1) <= lax.broadcasted_iota(jnp.int32, (BQ, BQ), 0)


def _head_rms(o, masks):
    o2 = o * o
    r = [lax.rsqrt(jnp.sum(jnp.where(mk, o2, 0.0), axis=1, keepdims=True) * (1.0 / FHD) + EPS) for mk in masks]
    return jnp.where(masks[0], r[0], r[1])


def _hosted(exchange):
    if exchange is None:
        return [], [], [], [], []
    return (exchange.inputs, [HBM] * len(exchange.inputs), [HBM] * len(exchange.out_shape), exchange.out_shape,
            exchange.sem_shapes())


def _fox_fwd(proj, gates, w2, exchange=None):
    ex_in, ex_in_specs, ex_out_specs, ex_out_shape, ex_scratch = _hosted(exchange)

    n_in = 3 * FOX_PAIRS + 2
    heads = [(pp, e) for pp in range(FOX_PAIRS) for e in range(2)]

    def body(*refs):
        qkv_refs, g_ref, w_ref = refs[:3 * FOX_PAIRS], refs[3 * FOX_PAIRS], refs[3 * FOX_PAIRS + 1]
        mix_ref, o_ref, lse_ref = refs[n_in + len(ex_in):n_in + 3 + len(ex_in)]
        ka_ref, vb_ref = refs[n_in + 3 + len(ex_in) + len(ex_out_shape):n_in + 5 + len(ex_in) + len(ex_out_shape)]
        ex_refs = refs[n_in:n_in + len(ex_in)] + refs[n_in + 3 + len(ex_in):n_in + 3 + len(ex_in) + len(ex_out_shape)] + refs[-2:]
        grp, qi = pl.program_id(0), pl.program_id(1)

        def head_index(pp, e):
            return 2 * (FOX_PAIRS * grp + pp) + e

        if exchange is not None:
            @pl.when((grp == 0) & (qi == 0))
            def _():
                exchange.start(*exchange.split(ex_refs))

        @pl.when(qi == 0)
        def _():
            gt = g_ref[...]
            for pp in range(FOX_PAIRS):
                kv = qkv_refs[3 * pp + 1][...]
                for e in range(2):
                    ka_ref[2 * pp + e] = _fox_operand(kv, e, _lane_col(gt, head_index(pp, e)), False)
                vb_ref[pp] = qkv_refs[3 * pp + 2][...].astype(BF16)

        masks = [_head_mask(0), _head_mask(1)]
        gt = g_ref[pl.ds(pl.multiple_of(qi * BQ, BQ), BQ), :]
        qs = [_fox_operand(qkv_refs[3 * pp][...], e, _lane_col(gt, head_index(pp, e)), True) for pp, e in heads]
        n = range(len(heads))

        def block(kj, carry, diagonal):
            rows = pl.ds(pl.multiple_of(kj * BQ, BQ), BQ)
            s = [_dot(qs[i], ka_ref[i, rows, :], 1, 1) for i in n]
            if diagonal:
                s = [jnp.where(_causal_block(), s[i], -jnp.inf) for i in n]
            m_new = [jnp.maximum(carry[i][0], jnp.max(s[i], axis=-1, keepdims=True)) for i in n]
            p = [jnp.exp(s[i] - m_new[i]) for i in n]
            alpha = [jnp.exp(carry[i][0] - m_new[i]) for i in n]
            l_new = [alpha[i] * carry[i][1] + jnp.sum(p[i], axis=-1, keepdims=True) for i in n]
            pv = [_dot(p[i], vb_ref[heads[i][0], rows, :]) for i in n]
            return tuple((m_new[i], l_new[i], alpha[i] * carry[i][2] + pv[i]) for i in n)

        one = (jnp.full((BQ, 1), -jnp.inf, F32), jnp.zeros((BQ, 1), F32), jnp.zeros((BQ, LANES), F32))
        below = lax.fori_loop(0, qi, lambda kj, carry: block(kj, carry, False), (one,) * len(heads))
        done = block(qi, below, True)
        for pp in range(FOX_PAIRS):
            (m0, l0, a0), (m1, l1, a1) = done[2 * pp], done[2 * pp + 1]
            o = jnp.where(masks[0], a0 / l0, a1 / l1)
            cols = slice(pp * LANES, (pp + 1) * LANES)
            o_ref[:, cols] = o
            mix_ref[:, cols] = (o * _head_rms(o, masks) * w_ref[...]).astype(BF16)
            lse_ref[pp] = _pair_cols(m0 + jnp.log(l0), m1 + jnp.log(l1))

        if exchange is not None:
            @pl.when((grp == NPAIR // FOX_PAIRS // 2) & (qi == 0))
            def _():
                exchange.middle(*exchange.split(ex_refs))

            @pl.when((grp == NPAIR // FOX_PAIRS - 1) & (qi == NQ - 1))
            def _():
                exchange.rest(*exchange.split(ex_refs))

    qkv_specs = []
    for pp in range(FOX_PAIRS):
        qkv_specs.append(pl.BlockSpec((BQ, LANES), lambda g, i, pp=pp: (i, 3 * (FOX_PAIRS * g + pp))))
        qkv_specs.append(pl.BlockSpec((S, LANES), lambda g, i, pp=pp: (0, 3 * (FOX_PAIRS * g + pp) + 1)))
        qkv_specs.append(pl.BlockSpec((S, LANES), lambda g, i, pp=pp: (0, 3 * (FOX_PAIRS * g + pp) + 2)))
    blk = pl.BlockSpec((BQ, FOX_PAIRS * LANES), lambda g, i: (i, g))
    res = pl.pallas_call(
        body, name="fox_fwd", grid=(NPAIR // FOX_PAIRS, NQ),
        in_specs=qkv_specs + [pl.BlockSpec((S, LANES), lambda g, i: (0, 0)), pl.BlockSpec((1, LANES), lambda g, i: (0, 0))]
        + ex_in_specs,
        out_specs=[blk, blk, pl.BlockSpec((FOX_PAIRS, BQ, 2), lambda g, i: (g, i, 0))] + ex_out_specs,
        out_shape=[jax.ShapeDtypeStruct((S, D), BF16), jax.ShapeDtypeStruct((S, DFOX), F32),
                   jax.ShapeDtypeStruct((NPAIR, S, 2), F32)] + ex_out_shape,
        scratch_shapes=[pltpu.VMEM((2 * FOX_PAIRS, S, LANES), BF16), pltpu.VMEM((FOX_PAIRS, S, LANES), BF16)] + ex_scratch,
        compiler_params=_cparams(),
    )(*([proj] * (3 * FOX_PAIRS)), gates, w2, *ex_in)
    return res[0], res[1], res[2], res[3:]


def _fox_norm_bwd(o, dmix, w2, exchange=None):
    ex_in, ex_in_specs, ex_out_specs, ex_out_shape, ex_scratch = _hosted(exchange)

    def body(*refs):
        o_ref, g_ref, w_ref = refs[:3]
        do_ref, dl_ref, dw_ref = refs[3 + len(ex_in):6 + len(ex_in)]
        ex_refs = refs[3:3 + len(ex_in)] + refs[6 + len(ex_in):]
        hp, qi = pl.program_id(0), pl.program_id(1)

        if exchange is not None:
            @pl.when((hp == 0) & (qi == 0))
            def _():
                exchange.start(*exchange.split(ex_refs))

        masks = [_head_mask(0), _head_mask(1)]
        ov = o_ref[...]
        g = g_ref[...]
        r = _head_rms(ov, masks)
        gw = g * w_ref[...]
        gwo = gw * ov
        mean = [jnp.sum(jnp.where(mk, gwo, 0.0), axis=1, keepdims=True) * (1.0 / FHD) for mk in masks]
        do = r * gw - ov * (r * r * r) * jnp.where(masks[0], mean[0], mean[1])
        do_ref[...] = do.astype(BF16)
        doo = do * ov
        dl_ref[...] = _pair_cols(*[jnp.sum(jnp.where(mk, doo, 0.0), axis=1, keepdims=True) for mk in masks])

        @pl.when((hp == 0) & (qi == 0))
        def _():
            dw_ref[...] = jnp.zeros_like(dw_ref)

        dw_ref[...] += jnp.sum(g * ov * r, axis=0, keepdims=True)

        @pl.when((hp == NPAIR - 1) & (qi == NQ - 1))
        def _():
            dw = dw_ref[...]
            dw_ref[...] = dw + pltpu.roll(dw, FHD, 1)
            if exchange is not None:
                exchange.finish(*exchange.split(ex_refs))

    blk = pl.BlockSpec((BQ, LANES), lambda hp, i: (i, hp))
    vec = pl.BlockSpec((1, LANES), lambda hp, i: (0, 0))
    res = pl.pallas_call(
        body, name="fox_norm_bwd", grid=(NPAIR, NQ), in_specs=[blk, blk, vec] + ex_in_specs,
        out_specs=[blk, pl.BlockSpec((None, BQ, 2), lambda hp, i: (hp, i, 0)), vec] + ex_out_specs,
        out_shape=[jax.ShapeDtypeStruct((S, DFOX), BF16), jax.ShapeDtypeStruct((NPAIR, S, 2), F32),
                   jax.ShapeDtypeStruct((1, LANES), F32)] + ex_out_shape,
        scratch_shapes=ex_scratch, compiler_params=_cparams(),
    )(o, dmix, w2, *ex_in)
    return res[0], res[1], res[2], res[3:]


def _fox_bwd(proj, do, gates, lse, delta, exchange=None):
    ex_in, ex_in_specs, ex_out_specs, ex_out_shape, ex_scratch = _hosted(exchange)

    pg = FOX_PAIRS_BWD
    n_in = 3 * pg + 4
    heads = [(pp, e) for pp in range(pg) for e in range(2)]

    def body(*refs):
        qkv_refs = refs[:3 * pg]
        do_ref, g_ref, lse_ref, dl_ref = refs[3 * pg:n_in]
        dproj_ref, dc_ref = refs[n_in + len(ex_in):n_in + 2 + len(ex_in)]
        qa_ref, dq_ref = refs[n_in + 2 + len(ex_in) + len(ex_out_shape):n_in + 4 + len(ex_in) + len(ex_out_shape)]
        ex_refs = refs[n_in:n_in + len(ex_in)] + refs[n_in + 2 + len(ex_in):n_in + 2 + len(ex_in) + len(ex_out_shape)] + refs[-2:]
        grp, kj = pl.program_id(0), pl.program_id(1)

        def head_index(pp, e):
            return 2 * (pg * grp + pp) + e

        if exchange is not None:
            @pl.when((grp == 0) & (kj == 0))
            def _():
                exchange.start(*exchange.split(ex_refs))

        @pl.when(kj == 0)
        def _():
            gt = g_ref[...]
            for pp in range(pg):
                qv = qkv_refs[3 * pp][...]
                for e in range(2):
                    qa_ref[2 * pp + e] = _fox_operand(qv, e, _lane_col(gt, head_index(pp, e)), True)
            dq_ref[...] = jnp.zeros_like(dq_ref)

        @pl.when((grp == 0) & (kj == 0))
        def _():
            dc_ref[...] = jnp.zeros_like(dc_ref)

        masks = [_head_mask(0), _head_mask(1)]
        krows = pl.ds(pl.multiple_of(kj * BQ, BQ), BQ)
        gk = g_ref[krows, :]
        kas = [_fox_operand(qkv_refs[3 * pp + 1][...], e, _lane_col(gk, head_index(pp, e)), False) for pp, e in heads]
        vbs = [qkv_refs[3 * pp + 2][...].astype(BF16) for pp in range(pg)]
        lane = lax.broadcasted_iota(jnp.int32, (BQ, LANES), 1)
        n = range(len(heads))

        def block(qi, carry, diagonal):
            dks, dvs, css = carry
            rows = pl.ds(pl.multiple_of(qi * BQ, BQ), BQ)
            qa = [qa_ref[i, rows, :] for i in n]
            s = [_dot(qa[i], kas[i], 1, 1) for i in n]
            if diagonal:
                s = [jnp.where(_causal_block(), s[i], -jnp.inf) for i in n]
            dov = [do_ref[rows, pp * LANES:(pp + 1) * LANES] for pp in range(pg)]
            doe = [jnp.where(masks[e], dov[pp], jnp.zeros_like(dov[pp])) for pp, e in heads]
            lse2 = [lse_ref[pp, rows, :] for pp in range(pg)]
            dl2 = [dl_ref[pp, rows, :] for pp in range(pg)]
            p = [jnp.exp(s[i] - _lane_col(lse2[heads[i][0]], heads[i][1])) for i in n]
            dp = [_dot(doe[i], vbs[heads[i][0]], 1, 1) for i in n]
            ds = [p[i] * (dp[i] - _lane_col(dl2[heads[i][0]], heads[i][1])) for i in n]
            dv_part = [_dot(p[i], doe[i], 0, 0) for i in n]
            dk_part = [_dot(ds[i], jnp.where(masks[heads[i][1]], qa[i], jnp.zeros_like(qa[i])), 0, 0) for i in n]
            dq_part = [jnp.where(masks[heads[i][1]], _dot(ds[i], kas[i]), 0.0) for i in n]
            css = tuple(css[i] + jnp.sum(ds[i], axis=0, keepdims=True) for i in n)
            dc = jnp.zeros((BQ, LANES), F32)
            for i in n:
                dc = dc + jnp.where(lane == head_index(*heads[i]), jnp.sum(ds[i], axis=1, keepdims=True), 0.0)
            for pp in range(pg):
                dq_ref[pp, rows, :] += (dq_part[2 * pp] + dq_part[2 * pp + 1]) * FOX_SCALE
            dc_ref[rows, :] += dc
            dks = tuple(dks[pp] + dk_part[2 * pp] + dk_part[2 * pp + 1] for pp in range(pg))
            dvs = tuple(dvs[pp] + dv_part[2 * pp] + dv_part[2 * pp + 1] for pp in range(pg))
            return dks, dvs, css

        zero = jnp.zeros((BQ, LANES), F32)
        first = block(kj, ((zero,) * pg, (zero,) * pg, (jnp.zeros((1, BQ), F32),) * len(heads)), True)
        dks, dvs, css = lax.fori_loop(kj + 1, NQ, lambda qi, carry: block(qi, carry, False), first)
        r = lax.broadcasted_iota(jnp.int32, (BQ, BQ), 0)
        c = lax.broadcasted_iota(jnp.int32, (BQ, BQ), 1)
        dcol = jnp.zeros((BQ, LANES), F32)
        for i in n:
            col = jnp.sum(jnp.where(r == c, css[i], 0.0), axis=1, keepdims=True)
            dcol = dcol + jnp.where(lane == head_index(*heads[i]), col, 0.0)
        dc_ref[krows, :] -= dcol
        for pp in range(pg):
            base = 3 * pp * LANES
            dproj_ref[krows, base + LANES:base + 2 * LANES] = dks[pp].astype(BF16)
            dproj_ref[krows, base + 2 * LANES:base + 3 * LANES] = dvs[pp].astype(BF16)

        @pl.when(kj == NQ - 1)
        def _():
            for pp in range(pg):
                dproj_ref[:, 3 * pp * LANES:(3 * pp + 1) * LANES] = dq_ref[pp].astype(BF16)

        if exchange is not None:
            @pl.when((grp == NPAIR // pg // 2) & (kj == 0))
            def _():
                exchange.middle(*exchange.split(ex_refs))

            @pl.when((grp == NPAIR // pg - 1) & (kj == NQ - 1))
            def _():
                exchange.rest(*exchange.split(ex_refs))

    qkv_specs = []
    for pp in range(pg):
        qkv_specs.append(pl.BlockSpec((S, LANES), lambda g, j, pp=pp: (0, 3 * (pg * g + pp))))
        qkv_specs.append(pl.BlockSpec((BQ, LANES), lambda g, j, pp=pp: (j, 3 * (pg * g + pp) + 1)))
        qkv_specs.append(pl.BlockSpec((BQ, LANES), lambda g, j, pp=pp: (j, 3 * (pg * g + pp) + 2)))
    pair = pl.BlockSpec((pg, S, 2), lambda g, j: (g, 0, 0))
    res = pl.pallas_call(
        body, name="fox_bwd", grid=(NPAIR // pg, NQ),
        in_specs=qkv_specs + [pl.BlockSpec((S, pg * LANES), lambda g, j: (0, g)), pl.BlockSpec((S, LANES), lambda g, j: (0, 0)),
                              pair, pair] + ex_in_specs,
        out_specs=[pl.BlockSpec((S, 3 * pg * LANES), lambda g, j: (0, g)), pl.BlockSpec((S, LANES), lambda g, j: (0, 0))]
        + ex_out_specs,
        out_shape=[jax.ShapeDtypeStruct((S, DPROJ_PAD), BF16), jax.ShapeDtypeStruct((S, LANES), F32)] + ex_out_shape,
        scratch_shapes=[pltpu.VMEM((2 * pg, S, LANES), BF16), pltpu.VMEM((pg, S, LANES), F32)] + ex_scratch,
        compiler_params=_cparams(),
    )(*([proj] * (3 * pg)), do, gates, lse, delta, *ex_in)
    return res[0], res[1], res[2:]


NQKV = 3 * NGH
GDN_QSCALE = GHD ** -0.5


def _shift_down(x, s):
    if s == 0:
        return x
    row = lax.broadcasted_iota(jnp.int32, x.shape, 0)
    return jnp.where(row >= s, pltpu.roll(x, s, 0), 0.0)


def _shift_up(x, s):
    if s == 0:
        return x
    n = x.shape[0]
    row = lax.broadcasted_iota(jnp.int32, x.shape, 0)
    return jnp.where(row < n - s, pltpu.roll(x, n - s, 0), 0.0)


def _conv_pre(xv, wv):
    pre = xv * wv[CONV_K - 1:CONV_K, :]
    for j in range(CONV_K - 1):
        pre = pre + _shift_down(xv, CONV_K - 1 - j) * wv[j:j + 1, :]
    return pre


def _l2_factors(b):
    return b < 2 * NGH, jnp.where(b < NGH, GDN_QSCALE, 1.0)


def _gdn_pre(proj, conv_w):
    def body(x_ref, w_ref, o_ref):
        b = pl.program_id(0)
        c = _silu(_conv_pre(x_ref[...], w_ref[...]))
        normed, scale = _l2_factors(b)
        rs = lax.rsqrt(jnp.sum(c * c, axis=-1, keepdims=True) + EPS)
        o_ref[...] = c * jnp.where(normed, rs, 1.0) * scale

    return pl.pallas_call(
        body, name="gdn_pre", grid=(NQKV,),
        in_specs=[pl.BlockSpec((S, GHD), lambda b: (0, BLK_GDN + b)), pl.BlockSpec((CONV_K, GHD), lambda b: (0, b))],
        out_specs=pl.BlockSpec((S, GHD), lambda b: (0, b)),
        out_shape=jax.ShapeDtypeStruct((S, NQKV * GHD), F32), compiler_params=_cparams(),
    )(proj, conv_w)


def _gdn_pre_bwd(proj, conv_w, dqkv, dproj):
    def body(x_ref, w_ref, dy_ref, dproj_in, dx_ref, dw_ref):
        del dproj_in
        b = pl.program_id(0)
        xv = x_ref[...]
        wv = w_ref[...]
        pre = _conv_pre(xv, wv)
        sig = _sigmoid(pre)
        c = pre * sig
        normed, scale = _l2_factors(b)
        g = dy_ref[...] * scale
        rs = lax.rsqrt(jnp.sum(c * c, axis=-1, keepdims=True) + EPS)
        dc_n = rs * g - c * (rs * rs * rs) * jnp.sum(g * c, axis=-1, keepdims=True)
        dc = jnp.where(normed, dc_n, g)
        dpre = dc * sig * (1.0 + pre * (1.0 - sig))
        dx = dpre * wv[CONV_K - 1:CONV_K, :]
        for j in range(CONV_K - 1):
            dx = dx + _shift_up(dpre, CONV_K - 1 - j) * wv[j:j + 1, :]
        dx_ref[...] = dx.astype(BF16)
        for j in range(CONV_K):
            dw_ref[j:j + 1, :] = jnp.sum(dpre * _shift_down(xv, CONV_K - 1 - j), axis=0, keepdims=True)

    return pl.pallas_call(
        body, name="gdn_pre_bwd", grid=(NQKV,),
        in_specs=[pl.BlockSpec((S, GHD), lambda b: (0, BLK_GDN + b)), pl.BlockSpec((CONV_K, GHD), lambda b: (0, b)),
                  pl.BlockSpec((None, S, GHD), lambda b: (b // NGH, 0, b % NGH)), pl.BlockSpec(memory_space=pl.ANY)],
        out_specs=[pl.BlockSpec((S, GHD), lambda b: (0, BLK_GDN + b)), pl.BlockSpec((CONV_K, GHD), lambda b: (0, b))],
        out_shape=[jax.ShapeDtypeStruct((S, DPROJ_PAD), BF16), jax.ShapeDtypeStruct((CONV_K, NQKV * GHD), F32)],
        input_output_aliases={3: 0}, compiler_params=_cparams(),
    )(proj, conv_w, dqkv, dproj)


CB = 16
NCB = NCH // CB


def _chunk_prep(qs, ks, vs, gcols, bcols, t_saved=None):
    n = range(len(qs))
    r = lax.broadcasted_iota(jnp.int32, (CHUNK, CHUNK), 0)
    c = lax.broadcasted_iota(jnp.int32, (CHUNK, CHUNK), 1)
    incl = c <= r
    eye = (r == c).astype(F32)
    grow = [jnp.sum(gcols[i] * eye, axis=0, keepdims=True) for i in n]
    gc_col = [jnp.sum(jnp.where(incl, grow[i], 0.0), axis=1, keepdims=True) for i in n]
    gc_row = [jnp.sum(jnp.where(r <= c, gcols[i], 0.0), axis=0, keepdims=True) for i in n]
    decay = [jnp.exp(jnp.where(incl, gc_col[i] - gc_row[i], -jnp.inf)) for i in n]
    kb = [ks[i] * bcols[i] for i in n]
    vb = [vs[i] * bcols[i] for i in n]
    kk = [_mm_nt(kb[i], ks[i]) for i in n]
    m = [jnp.where(c < r, kk[i] * decay[i], 0.0) for i in n]
    if t_saved is None:
        t_inv = [eye - m[i] for i in n]
        p = [_dot3(m[i], m[i]) for i in n]
        for step in range(5):
            t_inv = [t_inv[i] + _dot3(t_inv[i], p[i]) for i in n]
            if step < 4:
                p = [_dot3(p[i], p[i]) for i in n]
    else:
        t_inv = [_saved_inverse(m[i], t_saved[i]) for i in n]
    egc = [jnp.exp(gc_col[i]) for i in n]
    u = [_mm_nn(t_inv[i], vb[i]) for i in n]
    w = [_mm_nn(t_inv[i], kb[i] * egc[i]) for i in n]
    qk = [_mm_nt(qs[i], ks[i]) for i in n]
    gc_last = [gc_col[i][CHUNK - 1:CHUNK, :] for i in n]
    return [(u[i], w[i], qk[i] * decay[i], qs[i] * egc[i], ks[i] * jnp.exp(gc_last[i] - gc_col[i]), jnp.exp(gc_last[i]),
             t_inv[i]) for i in n]


def _prep_specs():
    rows = CB * CHUNK
    qs = pl.BlockSpec((rows, GHD), lambda i, h: (i, h))
    ks = pl.BlockSpec((rows, GHD), lambda i, h: (i, NGH + h))
    vs = pl.BlockSpec((rows, GHD), lambda i, h: (i, 2 * NGH + h))
    gs = pl.BlockSpec((rows, LANES), lambda i, h: (i, 0))
    a_s = pl.BlockSpec((None, rows, CHUNK), lambda i, h: (h, i, 0))
    gl_s = pl.BlockSpec((None, CB, 1, LANES), lambda i, h: (h, i, 0, 0))
    return qs, ks, vs, gs, a_s, gl_s


def _gdn_prep(qkv, gates, exchange=None):
    ex_in, ex_in_specs, ex_out_specs, ex_out_shape, ex_scratch = _hosted(exchange)

    def body(*refs):
        q_ref, k_ref, v_ref, g_ref = refs[:4]
        u_ref, w_ref, qd_ref, kd_ref, a_ref, gl_ref, t_ref = refs[4 + len(ex_in):11 + len(ex_in)]
        ex_refs = refs[4:4 + len(ex_in)] + refs[11 + len(ex_in):]
        h = pl.program_id(1)

        if exchange is not None:
            @pl.when((pl.program_id(0) == 0) & (h == 0))
            def _():
                exchange.start(*exchange.split(ex_refs))

        chunks = [pl.ds(cidx * CHUNK, CHUNK) for cidx in range(CB)]
        gts = [g_ref[rows, :] for rows in chunks]
        outs = _chunk_prep([q_ref[rows, :] for rows in chunks], [k_ref[rows, :] for rows in chunks],
                           [v_ref[rows, :] for rows in chunks], [_lane_col(gt, LANE_G + h) for gt in gts],
                           [_lane_col(gt, LANE_BETA + h) for gt in gts])
        for cidx, rows in enumerate(chunks):
            u, w, a, qd, kd, gl, t_inv = outs[cidx]
            u_ref[rows, :] = u
            w_ref[rows, :] = w
            qd_ref[rows, :] = qd
            kd_ref[rows, :] = kd
            a_ref[rows, :] = a
            t_ref[rows, :] = t_inv
            gl_ref[cidx] = jnp.broadcast_to(gl, (1, LANES))

        if exchange is not None:
            @pl.when((pl.program_id(0) == NCB // 2) & (h == 0))
            def _():
                exchange.middle(*exchange.split(ex_refs))

            @pl.when((pl.program_id(0) == NCB - 1) & (h == NGH - 1))
            def _():
                exchange.rest(*exchange.split(ex_refs))

    qs, ks, vs, gs, a_s, gl_s = _prep_specs()
    tok = jax.ShapeDtypeStruct((S, DGDN), F32)
    sq = jax.ShapeDtypeStruct((NGH, S, CHUNK), F32)
    res = pl.pallas_call(
        body, name="gdn_prep", grid=(NCB, NGH), in_specs=[qs, ks, vs, gs] + ex_in_specs,
        out_specs=[qs, qs, qs, qs, a_s, gl_s, a_s] + ex_out_specs,
        out_shape=[tok, tok, tok, tok, sq, jax.ShapeDtypeStruct((NGH, NCH, 1, LANES), F32), sq] + ex_out_shape,
        scratch_shapes=ex_scratch, compiler_params=_cparams(),
    )(qkv, qkv, qkv, gates, *ex_in)
    return res[:7], res[7:]


def _gdn_prep_bwd(qkv, gates, t_inv, du, dw, dqd, dkd, da, dgl, exchange=None):
    ex_in, ex_in_specs, ex_out_specs, ex_out_shape, ex_scratch = _hosted(exchange)

    def body(*refs):
        q_ref, k_ref, v_ref, g_ref, t_ref, du_ref, dw_ref, dqd_ref, dkd_ref, da_ref, dgl_ref = refs[:11]
        dqkv_ref, dg_ref = refs[11 + len(ex_in):13 + len(ex_in)]
        ex_refs = refs[11:11 + len(ex_in)] + refs[13 + len(ex_in):]
        h = pl.program_id(1)

        if exchange is not None:
            @pl.when((pl.program_id(0) == 0) & (h == 0))
            def _():
                exchange.start(*exchange.split(ex_refs))

        @pl.when(h == 0)
        def _():
            dg_ref[...] = jnp.zeros_like(dg_ref)

        lane = lax.broadcasted_iota(jnp.int32, (CHUNK, LANES), 1)
        chunks = [pl.ds(cidx * CHUNK, CHUNK) for cidx in range(CB)]
        gts = [g_ref[rows, :] for rows in chunks]
        t_saved = [t_ref[rows, :] for rows in chunks]
        _, vjp = jax.vjp(lambda *args: [o[:6] for o in _chunk_prep(*args, t_saved=t_saved)],
                         [q_ref[rows, :] for rows in chunks], [k_ref[rows, :] for rows in chunks],
                         [v_ref[rows, :] for rows in chunks], [_lane_col(gt, LANE_G + h) for gt in gts],
                         [_lane_col(gt, LANE_BETA + h) for gt in gts])
        dqs, dks, dvs, dgcs, dbcs = vjp([(du_ref[rows, :], dw_ref[rows, :], da_ref[rows, :], dqd_ref[rows, :],
                                          dkd_ref[rows, :], dgl_ref[cidx][:, 0:1]) for cidx, rows in enumerate(chunks)])
        for cidx, rows in enumerate(chunks):
            dq, dk, dv, dgc, dbc = dqs[cidx], dks[cidx], dvs[cidx], dgcs[cidx], dbcs[cidx]
            dqkv_ref[0, rows, :] = dq
            dqkv_ref[1, rows, :] = dk
            dqkv_ref[2, rows, :] = dv
            dg_ref[rows, :] += jnp.where(lane == LANE_G + h, dgc, 0.0) + jnp.where(lane == LANE_BETA + h, dbc, 0.0)

        if exchange is not None:
            @pl.when((pl.program_id(0) == NCB - 1) & (h == NGH - 1))
            def _():
                exchange.finish(*exchange.split(ex_refs))

    qs, ks, vs, gs, a_s, gl_s = _prep_specs()
    res = pl.pallas_call(
        body, name="gdn_prep_bwd", grid=(NCB, NGH), in_specs=[qs, ks, vs, gs, a_s, qs, qs, qs, qs, a_s, gl_s] + ex_in_specs,
        out_specs=[pl.BlockSpec((3, CB * CHUNK, GHD), lambda i, h: (0, i, h)), gs] + ex_out_specs,
        out_shape=[jax.ShapeDtypeStruct((3, S, DGDN), F32), jax.ShapeDtypeStruct((S, LANES), F32)] + ex_out_shape,
        scratch_shapes=ex_scratch, compiler_params=_cparams(),
    )(qkv, qkv, qkv, gates, t_inv, du, dw, dqd, dkd, da, dgl, *ex_in)
    return res[0], res[1], res[2:]


def _scan_specs(nh, parts, reverse):
    wide, rows, chunks = nh * GHD, S // parts, NCH // parts

    def part(p):
        return parts - 1 - p if reverse else p

    hs = pl.BlockSpec((rows, wide), lambda g, p: (part(p), g))
    a_s = pl.BlockSpec((nh, rows, CHUNK), lambda g, p: (g, part(p), 0))
    gl_s = pl.BlockSpec((nh, chunks, 1, LANES), lambda g, p: (g, part(p), 0, 0))
    st_s = pl.BlockSpec((nh, chunks, GHD, GHD), lambda g, p: (g, part(p), 0, 0))
    gz_s = pl.BlockSpec((rows, wide), lambda g, p: (part(p), BLK_GZ // nh + g))
    mix_s = pl.BlockSpec((rows, wide), lambda g, p: (part(p), NPAIR // nh + g))
    return hs, a_s, gl_s, st_s, gz_s, mix_s


def _head_cols(hh):
    return slice(hh * GHD, (hh + 1) * GHD)


SCAN_HEADS, SCAN_PARTS = 4, 2
SCAN_HEADS_BWD, SCAN_PARTS_BWD = 2, 2


def _gdn_scan(u, w, qd, kd, a, gl, proj, w_norm, mix):
    heads = range(SCAN_HEADS)

    def body(u_ref, w_ref, qd_ref, kd_ref, a_ref, gl_ref, z_ref, wn_ref, mix_in, mix_ref, o_ref, st_ref, carry_ref):
        del mix_in

        @pl.when(pl.program_id(1) == 0)
        def _():
            carry_ref[...] = jnp.zeros_like(carry_ref)

        def step(ci, states):
            rows = pl.ds(pl.multiple_of(ci * CHUNK, CHUNK), CHUNK)
            for hh in heads:
                st_ref[hh, ci] = states[hh]
            ws = [_dot(w_ref[rows, _head_cols(hh)], states[hh]) for hh in heads]
            qs = [_dot(qd_ref[rows, _head_cols(hh)], states[hh]) for hh in heads]
            vn = [u_ref[rows, _head_cols(hh)] - ws[hh] for hh in heads]
            av = [_dot(a_ref[hh, rows, :], vn[hh]) for hh in heads]
            kv = [_dot(kd_ref[rows, _head_cols(hh)], vn[hh], 0, 0) for hh in heads]
            for hh in heads:
                o_ref[rows, _head_cols(hh)] = qs[hh] + av[hh]
            return tuple(states[hh] * gl_ref[hh, ci] + kv[hh] for hh in heads)

        last = lax.fori_loop(0, NCH // SCAN_PARTS, step, tuple(carry_ref[hh] for hh in heads))
        for hh in heads:
            carry_ref[hh] = last[hh]
            ov = o_ref[:, _head_cols(hh)]
            mix_ref[:, _head_cols(hh)] = (ov * _rms_scale(ov) * wn_ref[...] * _silu(z_ref[:, _head_cols(hh)])).astype(BF16)

    hs, a_s, gl_s, st_s, gz_s, mix_s = _scan_specs(SCAN_HEADS, SCAN_PARTS, False)
    return pl.pallas_call(
        body, name="gdn_scan", grid=(NGH // SCAN_HEADS, SCAN_PARTS),
        in_specs=[hs, hs, hs, hs, a_s, gl_s, gz_s, pl.BlockSpec((1, GHD), lambda g, p: (0, 0)),
                  pl.BlockSpec(memory_space=pl.ANY)],
        out_specs=[mix_s, hs, st_s],
        out_shape=[jax.ShapeDtypeStruct((S, D), BF16), jax.ShapeDtypeStruct((S, DGDN), F32),
                   jax.ShapeDtypeStruct((NGH, NCH, GHD, GHD), F32)],
        input_output_aliases={8: 0}, scratch_shapes=[pltpu.VMEM((SCAN_HEADS, GHD, GHD), F32)], compiler_params=_cparams(),
    )(u, w, qd, kd, a, gl, proj, w_norm, mix)


def _gdn_scan_bwd(dmix, o, proj, w_norm, u, w, qd, kd, a, gl, states, dproj, exchange=None):
    ex_in, ex_in_specs, ex_out_specs, ex_out_shape, ex_scratch = _hosted(exchange)
    groups = NGH // SCAN_HEADS_BWD

    def body(*refs):
        dy_ref, o_ref, z_ref, wn_ref, u_ref, w_ref, qd_ref, kd_ref, a_ref, gl_ref, st_ref = refs[:11]
        dz_ref, du_ref, dw_ref, dqd_ref, dkd_ref, da_ref, dgl_ref, dwn_ref = refs[12 + len(ex_in):20 + len(ex_in)]
        do_ref, carry_ref = refs[20 + len(ex_in) + len(ex_out_shape):22 + len(ex_in) + len(ex_out_shape)]
        ex_refs = refs[12:12 + len(ex_in)] + refs[20 + len(ex_in):20 + len(ex_in) + len(ex_out_shape)] + refs[-2:]
        heads = range(SCAN_HEADS_BWD)
        chunks = NCH // SCAN_PARTS_BWD

        if exchange is not None:
            @pl.when((pl.program_id(0) == 0) & (pl.program_id(1) == 0))
            def _():
                exchange.start(*exchange.split(ex_refs))

        @pl.when((pl.program_id(0) == 0) & (pl.program_id(1) == 0))
        def _():
            dwn_ref[...] = jnp.zeros_like(dwn_ref)

        @pl.when(pl.program_id(1) == 0)
        def _():
            carry_ref[...] = jnp.zeros_like(carry_ref)

        wn = wn_ref[...]
        for hh in heads:
            c = _head_cols(hh)
            ov = o_ref[:, c]
            zv = z_ref[:, c]
            g = dy_ref[:, c]
            sig = _sigmoid(zv)
            dz_ref[:, c] = (g * (ov * _rms_scale(ov) * wn) * sig * (1.0 + zv * (1.0 - sig))).astype(BF16)
            do, dwt = _rms_bwd(ov, wn, g * zv * sig)
            do_ref[:, c] = do
            dwn_ref[...] += jnp.sum(dwt, axis=0, keepdims=True)

        def step(t, dstates):
            ci = chunks - 1 - t
            rows = pl.ds(pl.multiple_of(ci * CHUNK, CHUNK), CHUNK)
            cols = [_head_cols(hh) for hh in heads]
            state = [st_ref[hh, ci] for hh in heads]
            dov = [do_ref[rows, cols[hh]] for hh in heads]
            wv = [w_ref[rows, cols[hh]] for hh in heads]
            ws = [_dot(wv[hh], state[hh]) for hh in heads]
            adov = [_dot(a_ref[hh, rows, :], dov[hh], 0, 0) for hh in heads]
            kds = [_dot(kd_ref[rows, cols[hh]], dstates[hh]) for hh in heads]
            dqd = [_dot(dov[hh], state[hh], 1, 1) for hh in heads]
            qdo = [_dot(qd_ref[rows, cols[hh]], dov[hh], 0, 0) for hh in heads]
            vn = [u_ref[rows, cols[hh]] - ws[hh] for hh in heads]
            dvn = [adov[hh] + kds[hh] for hh in heads]
            da = [_dot(dov[hh], vn[hh], 1, 1) for hh in heads]
            dkd = [_dot(vn[hh], dstates[hh], 1, 1) for hh in heads]
            dwv = [_dot(dvn[hh], state[hh], 1, 1) for hh in heads]
            wdv = [_dot(wv[hh], dvn[hh], 0, 0) for hh in heads]
            for hh in heads:
                da_ref[hh, rows, :] = da[hh]
                dqd_ref[rows, cols[hh]] = dqd[hh]
                dkd_ref[rows, cols[hh]] = dkd[hh]
                dgl = jnp.sum(jnp.sum(dstates[hh] * state[hh], axis=1, keepdims=True), axis=0, keepdims=True)
                dgl_ref[hh, ci] = jnp.broadcast_to(dgl, (1, LANES))
                du_ref[rows, cols[hh]] = dvn[hh]
                dw_ref[rows, cols[hh]] = -dwv[hh]
            return tuple(dstates[hh] * gl_ref[hh, ci] + qdo[hh] - wdv[hh] for hh in heads)

        last = lax.fori_loop(0, chunks, step, tuple(carry_ref[hh] for hh in heads))
        for hh in heads:
            carry_ref[hh] = last[hh]

        if exchange is not None:
            @pl.when((pl.program_id(0) == groups - 1) & (pl.program_id(1) == SCAN_PARTS_BWD - 1))
            def _():
                exchange.finish(*exchange.split(ex_refs))

    hs, a_s, gl_s, st_s, gz_s, mix_s = _scan_specs(SCAN_HEADS_BWD, SCAN_PARTS_BWD, True)
    vec = pl.BlockSpec((1, GHD), lambda g, p: (0, 0))
    tok = jax.ShapeDtypeStruct((S, DGDN), F32)
    res = pl.pallas_call(
        body, name="gdn_scan_bwd", grid=(groups, SCAN_PARTS_BWD),
        in_specs=[mix_s, hs, gz_s, vec, hs, hs, hs, hs, a_s, gl_s, st_s, pl.BlockSpec(memory_space=pl.ANY)] + ex_in_specs,
        out_specs=[gz_s, hs, hs, hs, hs, a_s, gl_s, vec] + ex_out_specs,
        out_shape=[jax.ShapeDtypeStruct((S, DPROJ_PAD), BF16), tok, tok, tok, tok,
                   jax.ShapeDtypeStruct((NGH, S, CHUNK), F32), jax.ShapeDtypeStruct((NGH, NCH, 1, LANES), F32),
                   jax.ShapeDtypeStruct((1, GHD), F32)] + ex_out_shape,
        input_output_aliases={11: 0},
        scratch_shapes=[pltpu.VMEM((S // SCAN_PARTS_BWD, SCAN_HEADS_BWD * GHD), F32),
                        pltpu.VMEM((SCAN_HEADS_BWD, GHD, GHD), F32)] + ex_scratch,
        compiler_params=_cparams(),
    )(dmix, o, proj, w_norm, u, w, qd, kd, a, gl, states, dproj, *ex_in)
    return res[:8], res[8:]


def _place():
    return lax.axis_index("x"), lax.axis_index("y"), lax.axis_index("c")


def _other_chips(x, y):
    return [(1 - x, y), (x, 1 - y), (1 - x, 1 - y)]


HBM = pl.BlockSpec(memory_space=pltpu.HBM)
VMEM = pl.BlockSpec(memory_space=pltpu.VMEM)


def _half_rows(ref_or_rows, half):
    rows = ref_or_rows // 2
    return pl.ds(pl.multiple_of(half * rows, rows), rows)


class _Exchange:
    def __init__(self, inputs, out_shape, n_sems, start, finish=None, middle=None, rest=None):
        self.inputs, self.out_shape, self.n_sems, self.start = inputs, out_shape, n_sems, start
        if finish is None:
            def finish(*refs):
                middle(*refs)
                rest(*refs)
        self.finish = finish
        self.middle = middle if middle is not None else (lambda *refs: None)
        self.rest = rest if rest is not None else finish

    def sem_shapes(self):
        return [pltpu.SemaphoreType.DMA((self.n_sems,)), pltpu.SemaphoreType.DMA((self.n_sems,))]

    def split(self, refs):
        n_in, n_out = len(self.inputs), len(self.out_shape)
        return refs[:n_in], refs[n_in:n_in + n_out], refs[n_in + n_out], refs[n_in + n_out + 1]


def _run_exchange(ex, name):
    def body(*refs):
        parts = ex.split(refs)
        ex.start(*parts)
        ex.finish(*parts)

    return pl.pallas_call(
        body, name=name, in_specs=[HBM] * len(ex.inputs), out_specs=[HBM] * len(ex.out_shape), out_shape=ex.out_shape,
        scratch_shapes=ex.sem_shapes(), compiler_params=_cparams(),
    )(*ex.inputs)


def _allgather_exchange(shards, whole=()):
    n, nw = len(shards), len(whole)
    slots = 8

    def plan(src, outs, send_sems, recv_sems):
        x, y, c = _place()
        via_x, via_y, diagonal = _other_chips(x, y)
        id_x, id_y, id_diagonal = [2 * chip[0] + chip[1] for chip in (via_x, via_y, diagonal)]
        me, sibling = (x, y, c), (x, y, 1 - c)

        def rows_of(a, half, quarter):
            total = src[a].shape[0]
            if quarter is None:
                return _half_rows(total, half)
            return pl.ds(pl.multiple_of(half * (total // 2) + quarter * (total // 4), total // 4), total // 4)

        def copy(a, k, chip_index, half, quarter, to, from_src=False):
            rows = rows_of(a, half, quarter)
            dst = outs[a].at[chip_index, rows]
            return pltpu.make_async_remote_copy(
                src_ref=src[a].at[rows] if from_src else dst, dst_ref=dst, send_sem=send_sems.at[slots * a + k],
                recv_sem=recv_sems.at[slots * a + k], device_id=to, device_id_type=MESH)

        def whole_copy(b, k, chip_index, to):
            return pltpu.make_async_remote_copy(
                src_ref=src[n + b], dst_ref=outs[n + b].at[chip_index], send_sem=send_sems.at[slots * n + 3 * b + k],
                recv_sem=recv_sems.at[slots * n + 3 * b + k], device_id=to, device_id_type=MESH)

        first, stages, last = [], [], []
        for a in range(n):
            first += [copy(a, 0, 2 * x + y, c, None, (*via_x, c), True), copy(a, 1, 2 * x + y, c, None, (*via_y, c), True)]
            stages.append([
                (copy(a, 0, id_x, c, None, me),
                 [copy(a, 2, id_x, c, 0, (*via_y, c)), copy(a, 4, id_x, c, None, sibling)]),
                (copy(a, 1, id_y, c, None, me),
                 [copy(a, 3, id_y, c, 1, (*via_x, c)), copy(a, 5, id_y, c, None, sibling)]),
                (copy(a, 2, id_diagonal, c, 0, me), [copy(a, 6, id_diagonal, c, 0, sibling)]),
                (copy(a, 3, id_diagonal, c, 1, me), [copy(a, 7, id_diagonal, c, 1, sibling)]),
            ])
            last += [copy(a, 4, id_x, 1 - c, None, me), copy(a, 5, id_y, 1 - c, None, me),
                     copy(a, 6, id_diagonal, 1 - c, 0, me), copy(a, 7, id_diagonal, 1 - c, 1, me)]
        for b in range(nw):
            for k, (chip, index) in enumerate(((via_x, id_x), (via_y, id_y), (diagonal, id_diagonal))):
                first.append(whole_copy(b, k, 2 * x + y, (*chip, c)))
                last.append(whole_copy(b, k, index, me))
        return first, stages, last

    def start(*refs):
        for cp in plan(*refs)[0]:
            cp.start()

    def pass_on(stages, which):
        for stage in which:
            for per_shard in stages:
                lands, onward = per_shard[stage]
                lands.wait_recv()
                for cp in onward:
                    cp.start()

    def middle(*refs):
        pass_on(plan(*refs)[1], (0, 1))

    def rest(*refs):
        first, stages, last = plan(*refs)
        pass_on(stages, (2, 3))
        for cp in last:
            cp.wait_recv()
        for cp in first + [cp for per_shard in stages for _, onward in per_shard for cp in onward]:
            cp.wait_send()

    out_shape = [jax.ShapeDtypeStruct((NCHIP,) + s.shape, s.dtype) for s in list(shards) + list(whole)]
    return _Exchange(list(shards) + list(whole), out_shape, slots * n + 3 * nw, start, middle=middle, rest=rest)


def _with_own(gathered, own):
    x, y, _ = _place()
    return lax.dynamic_update_index_in_dim(gathered, own, 2 * x + y, axis=0)


def _simple_exchange(inputs, out_shape, copies_of):
    def start(*refs):
        for cp in copies_of(*refs):
            cp.start()

    def finish(*refs):
        for cp in copies_of(*refs):
            cp.wait()

    return _Exchange(list(inputs), out_shape, len(out_shape) * 3, start, finish)


def _pair_exchange(grads):
    def copies_of(src, outs, send_sems, recv_sems):
        x, y, c = _place()
        return [pltpu.make_async_remote_copy(
            src_ref=src[a].at[:, _half_rows(src[a].shape[1], 1 - c)], dst_ref=outs[a], send_sem=send_sems.at[a],
            recv_sem=recv_sems.at[a], device_id=(x, y, 1 - c), device_id_type=MESH) for a in range(len(src))]

    return _simple_exchange(
        grads, [jax.ShapeDtypeStruct((g.shape[0], g.shape[1] // 2, g.shape[2]), g.dtype) for g in grads], copies_of)


def _pair_sum(grads, theirs, name):
    n = len(grads)

    def body(*refs):
        south = lax.axis_index("c") == 0
        for a in range(n):
            g = refs[a][...]
            half = g.shape[0] // 2
            mine = jnp.where(south, g[:half], g[half:])
            refs[2 * n + a][...] = (mine.astype(F32) + refs[n + a][...].astype(F32)).astype(BF16)

    def specs(arrs):
        return [pl.BlockSpec((None,) + g.shape[1:], lambda j: (j, 0, 0)) for g in arrs]

    return pl.pallas_call(
        body, name=name, grid=(NCHIP,), in_specs=specs(grads) + specs(theirs), out_specs=specs(theirs),
        out_shape=[jax.ShapeDtypeStruct(g.shape, BF16) for g in theirs], compiler_params=_cparams(),
    )(*grads, *theirs)


def _chip_exchange(parts):
    n = len(parts)
    slots = 6

    def plan(src, outs, send_sems, recv_sems):
        x, y, c = _place()
        via_x, via_y, diagonal = _other_chips(x, y)
        id_x, id_y, id_diagonal = [2 * chip[0] + chip[1] for chip in (via_x, via_y, diagonal)]
        legs, direct, relays = [], [], []
        for a in range(n):
            received, passing = outs[a], outs[n + a]
            half = src[a].shape[1] // 2

            def copy(k, source, target, to, a=a):
                return pltpu.make_async_remote_copy(
                    src_ref=source, dst_ref=target, send_sem=send_sems.at[slots * a + k],
                    recv_sem=recv_sems.at[slots * a + k], device_id=(*to, c), device_id_type=MESH)

            legs += [copy(2, src[a].at[id_diagonal, 0:half], passing.at[0], via_x),
                     copy(3, src[a].at[id_diagonal, half:2 * half], passing.at[1], via_y)]
            direct += [copy(0, src[a].at[id_x], received.at[0], via_x), copy(1, src[a].at[id_y], received.at[1], via_y)]
            relays += [copy(4, passing.at[0], received.at[2, 0:half], via_y),
                       copy(5, passing.at[1], received.at[2, half:2 * half], via_x)]
        return legs, direct, relays

    def start(*refs):
        legs, direct, _ = plan(*refs)
        for cp in legs + direct:
            cp.start()

    def middle(*refs):
        legs, _, relays = plan(*refs)
        for lands, onward in zip(legs, relays):
            lands.wait_recv()
            onward.start()

    def rest(*refs):
        legs, direct, relays = plan(*refs)
        for cp in direct + relays:
            cp.wait_recv()
        for cp in legs + direct + relays:
            cp.wait_send()

    out_shape = [jax.ShapeDtypeStruct((NCHIP - 1,) + p.shape[1:], p.dtype) for p in parts]
    out_shape += [jax.ShapeDtypeStruct((2, p.shape[1] // 2, p.shape[2]), p.dtype) for p in parts]
    return _Exchange(list(parts), out_shape, slots * n, start, middle=middle, rest=rest)


def _chip_sum(parts, received, exchange=None):
    n = len(parts)
    steps = 4
    ex_in, ex_in_specs, ex_out_specs, ex_out_shape, ex_scratch = _hosted(exchange)

    def body(*refs):
        ex_refs = refs[2 * n:2 * n + len(ex_in)] + refs[3 * n + len(ex_in):]
        if exchange is not None:
            @pl.when(pl.program_id(0) == 0)
            def _():
                exchange.start(*exchange.split(ex_refs))

        chip = 2 * lax.axis_index("x") + lax.axis_index("y")
        for a in range(n):
            p, r = refs[a], refs[n + a]
            own = jnp.where(chip == 0, p[0], jnp.where(chip == 1, p[1], jnp.where(chip == 2, p[2], p[3])))
            refs[2 * n + len(ex_in) + a][...] = ((own.astype(F32) + r[0].astype(F32)) + r[1].astype(F32)) + r[2].astype(F32)

        if exchange is not None:
            @pl.when(pl.program_id(0) == steps - 1)
            def _():
                exchange.finish(*exchange.split(ex_refs))

    def specs(arrs):
        return [pl.BlockSpec((g.shape[0], g.shape[1] // steps, g.shape[2]), lambda i: (0, i, 0)) for g in arrs]

    out_specs = [pl.BlockSpec((g.shape[1] // steps, g.shape[2]), lambda i: (i, 0)) for g in parts]
    res = pl.pallas_call(
        body, name="grads_chip_sum", grid=(steps,), in_specs=specs(parts) + specs(received) + ex_in_specs,
        out_specs=out_specs + ex_out_specs, out_shape=[jax.ShapeDtypeStruct(g.shape[1:], F32) for g in parts] + ex_out_shape,
        scratch_shapes=ex_scratch, compiler_params=_cparams(),
    )(*parts, *received, *ex_in)
    return res[:n], res[n:]


def _pair_share(halves):
    def copies_of(src, outs, send_sems, recv_sems):
        x, y, c = _place()
        return [pltpu.make_async_remote_copy(
            src_ref=src[a], dst_ref=outs[a], send_sem=send_sems.at[a], recv_sem=recv_sems.at[a],
            device_id=(x, y, 1 - c), device_id_type=MESH) for a in range(len(src))]

    return _simple_exchange(halves, [jax.ShapeDtypeStruct(h.shape, F32) for h in halves], copies_of)


def _adamw_math(w, g, m, v):
    nm = ADAM_B1 * m + (1.0 - ADAM_B1) * g
    nv = ADAM_B2 * v + (1.0 - ADAM_B2) * jnp.square(g)
    m_hat = nm / (1.0 - ADAM_B1 ** ADAM_STEP)
    v_hat = nv / (1.0 - ADAM_B2 ** ADAM_STEP)
    return -ADAM_LR * (m_hat / (jnp.sqrt(v_hat) + ADAM_EPS) + ADAM_WD * w), nm, nv


def _adamw_big(ws, g_mine, g_theirs, ms, vs, exchange=None):
    n = len(ws)
    steps = 8
    ex_in, ex_in_specs, ex_out_specs, ex_out_shape, ex_scratch = _hosted(exchange)

    def body(*refs):
        ex_refs = refs[5 * n:5 * n + len(ex_in)] + refs[9 * n + len(ex_in):]
        outs = refs[5 * n + len(ex_in):9 * n + len(ex_in)]
        if exchange is not None:
            @pl.when(pl.program_id(0) == 0)
            def _():
                exchange.start(*exchange.split(ex_refs))

        own_half = (pl.program_id(0) // (steps // 2)) == lax.axis_index("c")
        for a in range(n):
            g = jnp.where(own_half, refs[n + a][...], refs[2 * n + a][...])
            d, nm, nv = _adamw_math(refs[a][...], g, refs[3 * n + a][...], refs[4 * n + a][...])
            outs[a][...] = g
            outs[n + a][...] = d
            outs[2 * n + a][...] = nm
            outs[3 * n + a][...] = nv

        if exchange is not None:
            @pl.when(pl.program_id(0) == steps - 1)
            def _():
                exchange.finish(*exchange.split(ex_refs))

    specs = [pl.BlockSpec((w.shape[0] // steps, w.shape[1]), lambda i: (i, 0)) for w in ws]
    half_specs = [pl.BlockSpec((g.shape[0] // (steps // 2), g.shape[1]), lambda i: (i % (steps // 2), 0)) for g in g_mine]
    shapes = [jax.ShapeDtypeStruct(w.shape, F32) for w in ws]
    res = pl.pallas_call(
        body, name="adamw_big", grid=(steps,), in_specs=specs + half_specs * 2 + specs * 2 + ex_in_specs,
        out_specs=specs * 4 + ex_out_specs, out_shape=shapes * 4 + ex_out_shape, scratch_shapes=ex_scratch,
        compiler_params=_cparams(),
    )(*ws, *g_mine, *g_theirs, *ms, *vs, *ex_in)
    return res[:n], res[n:2 * n], res[2 * n:3 * n], res[3 * n:4 * n], res[4 * n:]


def _adamw_in(w, g_mine, g_theirs, m, v):
    half = D // 2

    def body(w_ref, gm_ref, gt_ref, m_ref, v_ref, g_out, d_out, nm_out, nv_out, g_ref):
        south = lax.axis_index("c") == 0
        g_ref[0:half, :] = jnp.where(south, gm_ref[...], gt_ref[...])
        g_ref[half:D, :] = jnp.where(south, gt_ref[...], gm_ref[...])
        g = g_ref[0:CW, :]
        d, nm, nv = _adamw_math(w_ref[...], g, m_ref[...], v_ref[...])
        g_out[...] = g
        d_out[...] = d
        nm_out[...] = nm
        nv_out[...] = nv

    spec = pl.BlockSpec((CW, LANES), lambda i: (0, i))
    half_spec = pl.BlockSpec((half, LANES), lambda i: (0, i))
    return pl.pallas_call(
        body, name="adamw_in", grid=(D // LANES,), in_specs=[spec, half_spec, half_spec, spec, spec], out_specs=[spec] * 4,
        out_shape=[jax.ShapeDtypeStruct((CW, D), F32)] * 4, scratch_shapes=[pltpu.VMEM((D, LANES), F32)],
        compiler_params=_cparams(),
    )(w, g_mine, g_theirs, m, v)


NORM_NAMES = ("pre_mix_norm", "post_mix_norm", "pre_mlp_norm", "post_mlp_norm")
SMALL_NAMES = NORM_NAMES + ("gdn_conv_w", "fox_f_bias", "gdn_dt_bias", "gdn_a_log", "fox_out_norm", "gdn_out_norm")
CONV_COLS = 3 * DGDN // NCHIP


def _small_gather(d_norms, d_conv, sums, d_fox_norm, d_gdn_norm, loss_row):
    n_arrays = 6
    n_remote = n_arrays * (NDEV - 1)

    def copies_of(src, outs, send_sems, recv_sems):
        x, y, c = _place()
        me = 4 * x + 2 * y + c

        def from_me(chip_index):
            cols = pl.ds(pl.multiple_of(chip_index * CONV_COLS, LANES), CONV_COLS)
            return [src[0], src[1].at[:, cols], src[2], src[3], src[4], src[5]]

        local = [pltpu.make_async_copy(s, outs[a].at[me], send_sems.at[n_remote + a]) for a, s in enumerate(from_me(2 * x + y))]
        remote = []
        for k in range(1, NDEV):
            px, py, pc = x ^ ((k >> 2) & 1), y ^ ((k >> 1) & 1), c ^ (k & 1)
            remote += [pltpu.make_async_remote_copy(
                src_ref=s, dst_ref=outs[a].at[me], send_sem=send_sems.at[n_arrays * (k - 1) + a],
                recv_sem=recv_sems.at[n_arrays * (k - 1) + a], device_id=(px, py, pc), device_id_type=MESH)
                for a, s in enumerate(from_me(2 * px + py))]
        return local + remote

    def start(*refs):
        for cp in copies_of(*refs):
            cp.start()

    def finish(*refs):
        for cp in copies_of(*refs):
            cp.wait()

    shapes = [(4, D), (CONV_K, CONV_COLS), (8, LANES), (1, LANES), (1, LANES), (1, LANES)]
    return _Exchange([d_norms, d_conv, sums, d_fox_norm, d_gdn_norm, loss_row],
                     [jax.ShapeDtypeStruct((NDEV,) + s, F32) for s in shapes], n_remote + n_arrays, start, finish)


def _small_adamw(gathered, ws, ms, vs):
    n = len(SMALL_NAMES)
    ng = len(gathered)

    def body(*refs):
        def total(buf):
            acc = buf[0]
            for i in range(1, NDEV):
                acc = acc + buf[i]
            return acc

        t_norms, t_conv, t_sums, t_fn, t_gn, t_loss = [total(r) for r in refs[:ng]]
        w_refs, m_refs, v_refs = refs[ng:ng + n], refs[ng + n:ng + 2 * n], refs[ng + 2 * n:ng + 3 * n]
        outs = refs[ng + 3 * n:]
        outs[4 * n][...] = t_loss
        grads = [t_norms[i:i + 1, :] for i in range(4)] + [
            t_conv, t_sums[0:1, 0:NFH], t_sums[1:2, 0:NGH], t_sums[2:3, 0:NGH], t_fn[:, 0:FHD], t_gn]
        for a in range(n):
            d, nm, nv = _adamw_math(w_refs[a][...], grads[a], m_refs[a][...], v_refs[a][...])
            outs[a][...] = grads[a]
            outs[n + a][...] = d
            outs[2 * n + a][...] = nm
            outs[3 * n + a][...] = nv

    def whole(arr):
        return pl.BlockSpec(arr.shape, lambda i: (0,) * arr.ndim)

    res = pl.pallas_call(
        body, name="small_adamw", grid=(1,), in_specs=[whole(t) for t in gathered] + [whole(w) for w in ws] * 3,
        out_specs=[whole(w) for w in ws] * 4 + [pl.BlockSpec((1, LANES), lambda i: (0, 0))],
        out_shape=[jax.ShapeDtypeStruct(w.shape, F32) for w in ws] * 4 + [jax.ShapeDtypeStruct((1, LANES), F32)],
        compiler_params=_cparams(),
    )(*gathered, *ws, *ms, *vs)
    return res[:n], res[n:2 * n], res[2 * n:3 * n], res[3 * n:4 * n], res[4 * n]


CW = DPROJ // NCHIP
PROJ_RUNS = tuple((part * DFOX + hp * LANES, part * DFOX + (hp + 1) * LANES, (3 * hp + part) * LANES)
                  for hp in range(NPAIR) for part in range(3)) + (
    (1536, 1544, BLK_SMALL * LANES), (1544, 3080, BLK_GDN * LANES), (3080, 3088, BLK_SMALL * LANES + 8),
    (3088, 3600, BLK_GZ * LANES))


def _proj_pieces():
    pieces = []
    for lo, hi, at in PROJ_RUNS:
        while lo < hi:
            j = lo // CW
            end = min(hi, (j + 1) * CW)
            pieces.append((j, lo - j * CW, at, end - lo))
            at, lo = at + end - lo, end
    return pieces


RT = 256


def _to_padded_rows(gathered):
    def body(src_ref, out_ref, blocks_ref, rows_ref):
        blocks_ref[...] = src_ref[...].astype(F32)
        rows_ref[...] = jnp.zeros_like(rows_ref)
        for j, start, at, n in _proj_pieces():
            rows_ref[at:at + n, :] = blocks_ref[j, start:start + n, :]
        out_ref[...] = rows_ref[...].astype(out_ref.dtype)

    return pl.pallas_call(
        body, name="proj_rows_in", grid=(D // RT,), in_specs=[pl.BlockSpec((NCHIP, D, RT), lambda i: (0, 0, i))],
        out_specs=pl.BlockSpec((DPROJ_PAD, RT), lambda i: (0, i)), out_shape=jax.ShapeDtypeStruct((DPROJ_PAD, D), gathered.dtype),
        scratch_shapes=[pltpu.VMEM((NCHIP, D, RT), F32), pltpu.VMEM((DPROJ_PAD, RT), F32)], compiler_params=_cparams(),
    )(gathered)


def _from_padded_rows(w):
    def body(src_ref, out_ref, rows_ref, blocks_ref):
        rows_ref[...] = src_ref[...].astype(F32)
        blocks_ref[...] = jnp.zeros_like(blocks_ref)
        for j, start, at, n in _proj_pieces():
            blocks_ref[j, start:start + n, :] = rows_ref[at:at + n, :]
        out_ref[...] = blocks_ref[...].astype(out_ref.dtype)

    return pl.pallas_call(
        body, name="proj_rows_out", grid=(D // RT,), in_specs=[pl.BlockSpec((DPROJ_PAD, RT), lambda i: (0, i))],
        out_specs=pl.BlockSpec((NCHIP, D, RT), lambda i: (0, 0, i)), out_shape=jax.ShapeDtypeStruct((NCHIP, D, D), w.dtype),
        scratch_shapes=[pltpu.VMEM((DPROJ_PAD, RT), F32), pltpu.VMEM((NCHIP, D, RT), F32)], compiler_params=_cparams(),
    )(w)


def _local_step(x, target, first_weights, late_weights, reduce_late, reduce_in, pre_mix_norm, fox_f_bias, fox_out_norm,
                gdn_a_log, gdn_dt_bias, gdn_out_norm, post_mix_norm, pre_mlp_norm, post_mlp_norm):
    bias_vec = jnp.zeros((1, LANES), F32).at[0, 0:NFH].set(fox_f_bias).at[0, LANE_G:LANE_G + NGH].set(gdn_dt_bias)
    alog_vec = jnp.zeros((1, LANES), F32).at[0, LANE_G:LANE_G + NGH].set(gdn_a_log)
    w2 = jnp.concatenate([fox_out_norm, fox_out_norm], axis=1)

    h, first = _pre_norm(x, pre_mix_norm, exchange=first_weights[0])
    win_p, conv_w = first_weights[1](first)
    proj = _matmul(h, win_p, tb=True, tm=2048, tn=768, tk=1024, name="mm_proj", exchange=late_weights[0])
    proj, late_a = proj if late_weights[0] is not None else (proj, [])
    gates = _gates(proj, bias_vec, alog_vec)
    mix, fox_o, lse, late_b = _fox_fwd(proj, gates, w2, exchange=late_weights[1])
    qkv = _gdn_pre(proj, conv_w)
    (u, w, qd, kd, a_intra, gl, t_inv), _ = _gdn_prep(qkv, gates)
    wout, wup3 = late_weights[3](late_a, late_b)
    mix, gdn_raw, states = _gdn_scan(u, w, qd, kd, a_intra, gl, proj, gdn_out_norm, mix)
    mixed = _matmul(mix, wout, tm=2048, tk=1024, name="mm_out")
    x1, h2 = _post_mix(x, mixed, post_mix_norm, pre_mlp_norm)

    def relu2(acc):
        r = jnp.maximum(acc, 0.0)
        return r, r * r

    up_act = _matmul(h2, wup3, b3=True, tm=1024, tn=1024, tk=1024, out_dtypes=(BF16, BF16), epilogue=relu2,
                     name="mm_up", exchange=late_weights[2])
    (up_relu, act), late_c = up_act if late_weights[2] is not None else (up_act, [])
    wdown = late_weights[4](late_c)
    y = _matmul(act, wdown, tm=1024, tk=DFF, name="mm_down")
    dx2, dy, d_post_mlp, loss_row = _loss_head(x1, y, post_mlp_norm, target)

    dwdown = _matmul(act, dy, ta=True, tm=1024, tn=1024, tk=2048, out_dtypes=(BF16,), name="mm_dwdown")

    def relu2_bwd(acc, r):
        return (acc * 2.0 * r.astype(F32),)

    dup = _matmul(dy, wdown, tb=True, tm=1024, tn=1024, tk=1024, out_dtypes=(BF16,), extra=(up_relu,), epilogue=relu2_bwd,
                  name="mm_dact")
    dwup3 = _matmul(h2, dup, ta=True, tm=1024, tn=1024, tk=2048, out_dtypes=(BF16,), o3=True, name="mm_dwup")
    dh2 = _matmul(dup, wup3, tb=True, b3=True, tm=1024, tk=DFF, name="mm_dh2")
    dx1, dmixed, d_pre_mlp, d_post_mix = _mid_bwd(dh2, x1, pre_mlp_norm, dx2, mixed, post_mix_norm)
    dwout = _matmul(mix, dmixed, ta=True, tm=1024, tn=1024, tk=2048, out_dtypes=(BF16,), name="mm_dwout")
    dmix = _matmul(dmixed, wout, tb=True, tm=2048, tk=1024, name="mm_dmix")

    dfox, delta, d_fox_norm, from_sibling = _fox_norm_bwd(fox_o, dmix, w2, exchange=reduce_late[0](dwout, dwup3, dwdown))
    dproj, dcum_fox, reduced_a = _fox_bwd(proj, dfox, gates, lse, delta, exchange=reduce_late[1](from_sibling))
    (dproj, du, dw, dqd, dkd, da, dgl, d_gdn_norm), reduced_b = _gdn_scan_bwd(
        dmix, gdn_raw, proj, gdn_out_norm, u, w, qd, kd, a_intra, gl, states, dproj, exchange=reduce_late[2]())
    dqkv, dgates_gdn, reduced_c = _gdn_prep_bwd(qkv, gates, t_inv, du, dw, dqd, dkd, da, dgl, exchange=reduce_late[3]())
    reduced_late = (reduced_a, reduced_b, reduced_c)
    dproj, d_conv = _gdn_pre_bwd(proj, conv_w, dqkv, dproj)
    dproj, sums = _gates_bwd(proj, bias_vec, alog_vec, dgates_gdn, dcum_fox, dproj)

    dwin_p = _matmul(dproj, h, ta=True, tm=1280, tn=1024, tk=2048, out_dtypes=(BF16,), name="mm_dwin")
    exchange_in = reduce_in(dwin_p)
    dh = _matmul(dproj, win_p, tm=1024, tk=DPROJ_PAD, name="mm_dh", exchange=exchange_in)
    dh, reduced_in = dh if exchange_in is not None else (dh, [])
    grad_x, d_pre_mix = _pre_norm_bwd(dh, x, pre_mix_norm, dx1)

    d_norms = jnp.concatenate([d_pre_mix, d_post_mix, d_pre_mlp, d_post_mlp], axis=0)
    return grad_x, (d_norms, d_conv, sums, d_fox_norm, d_gdn_norm, loss_row), reduced_late, reduced_in


def kernel(x, pre_mix_norm, w_in, fox_f_bias, fox_out_norm, gdn_conv_w, gdn_a_log, gdn_dt_bias, gdn_out_norm, w_out, post_mix_norm, pre_mlp_norm, w_up, w_down, post_mlp_norm, loss_target, m_pre_mix_norm, m_w_in, m_fox_f_bias, m_fox_out_norm, m_gdn_conv_w, m_gdn_a_log, m_gdn_dt_bias, m_gdn_out_norm, m_w_out, m_post_mix_norm, m_pre_mlp_norm, m_w_up, m_w_down, m_post_mlp_norm, v_pre_mix_norm, v_w_in, v_fox_f_bias, v_fox_out_norm, v_gdn_conv_w, v_gdn_a_log, v_gdn_dt_bias, v_gdn_out_norm, v_w_out, v_post_mix_norm, v_pre_mlp_norm, v_w_up, v_w_down, v_post_mlp_norm):
    weights = dict(pre_mix_norm=pre_mix_norm, w_in=w_in, fox_f_bias=fox_f_bias, fox_out_norm=fox_out_norm, gdn_conv_w=gdn_conv_w,
                   gdn_a_log=gdn_a_log, gdn_dt_bias=gdn_dt_bias, gdn_out_norm=gdn_out_norm, w_out=w_out, post_mix_norm=post_mix_norm,
                   pre_mlp_norm=pre_mlp_norm, w_up=w_up, w_down=w_down, post_mlp_norm=post_mlp_norm)
    m_in = dict(pre_mix_norm=m_pre_mix_norm, w_in=m_w_in, fox_f_bias=m_fox_f_bias, fox_out_norm=m_fox_out_norm, gdn_conv_w=m_gdn_conv_w,
                gdn_a_log=m_gdn_a_log, gdn_dt_bias=m_gdn_dt_bias, gdn_out_norm=m_gdn_out_norm, w_out=m_w_out, post_mix_norm=m_post_mix_norm,
                pre_mlp_norm=m_pre_mlp_norm, w_up=m_w_up, w_down=m_w_down, post_mlp_norm=m_post_mlp_norm)
    v_in = dict(pre_mix_norm=v_pre_mix_norm, w_in=v_w_in, fox_f_bias=v_fox_f_bias, fox_out_norm=v_fox_out_norm, gdn_conv_w=v_gdn_conv_w,
                gdn_a_log=v_gdn_a_log, gdn_dt_bias=v_gdn_dt_bias, gdn_out_norm=v_gdn_out_norm, w_out=v_w_out, post_mix_norm=v_post_mix_norm,
                pre_mlp_norm=v_pre_mlp_norm, w_up=v_w_up, w_down=v_w_down, post_mlp_norm=v_post_mlp_norm)
    order_w = ("pre_mix_norm", "w_in", "fox_f_bias", "fox_out_norm", "gdn_conv_w", "gdn_a_log", "gdn_dt_bias", "gdn_out_norm", "w_out",
               "post_mix_norm", "pre_mlp_norm", "w_up", "w_down", "post_mlp_norm")
    big = ("w_in", "w_out", "w_up", "w_down")

    def row(v):
        return v if v.ndim == 2 else v.reshape(1, -1)

    win_shard = jnp.pad(w_in.T.astype(BF16), ((0, D - CW), (0, 0)))

    def resolve_first(gathered):
        win_g, conv_g = gathered
        return (_to_padded_rows(_with_own(win_g, win_shard)),
                _with_own(conv_g, gdn_conv_w).transpose(1, 0, 2).reshape(CONV_K, 3 * DGDN))

    late_shards = [weights[n].astype(BF16) for n in big[1:]]

    gathered_down = []

    def resolve_out_up(gathered_out, gathered_mlp):
        gathered_down.append(gathered_mlp[1])
        return _with_own(gathered_out[0], late_shards[0]).reshape(D, D), _with_own(gathered_mlp[0], late_shards[1])

    def resolve_down(_):
        return _with_own(gathered_down[0], late_shards[2]).reshape(DFF, D)

    pair_sums, late_blocks = {}, []

    def pair_summed(names, blocks, theirs):
        for n, s in zip(names, _pair_sum(blocks, theirs, "grads_pair_sum_" + names[0])):
            pair_sums[n] = s

    def late_pair_exchange(dwout, dwup3, dwdown):
        late_blocks.extend([dwout.reshape(NCHIP, D // NCHIP, D), dwup3, dwdown.reshape(NCHIP, DFF // NCHIP, D)])
        return _pair_exchange(late_blocks)

    def late_chip_exchange(theirs):
        pair_summed(big[1:], late_blocks, theirs)
        return _chip_exchange([pair_sums["w_up"], pair_sums["w_down"]])

    def reduce_in(dwin_p):
        blocks = [_from_padded_rows(dwin_p)]
        pair_summed(big[:1], blocks, _run_exchange(_pair_exchange(blocks), "grads_pair_exchange_w_in"))
        return _chip_exchange([pair_sums["w_in"]])

    grad_x, small, received_late, received_in = _local_step(
        x[0], loss_target[0], (_allgather_exchange([win_shard], whole=[gdn_conv_w]), resolve_first),
        (_allgather_exchange(late_shards[:1]), _allgather_exchange(late_shards[1:]), None, resolve_out_up, resolve_down),
        (late_pair_exchange, late_chip_exchange, lambda: None, lambda: _chip_exchange([pair_sums["w_out"]])),
        reduce_in, row(pre_mix_norm), fox_f_bias, row(fox_out_norm), gdn_a_log, gdn_dt_bias,
        row(gdn_out_norm), row(post_mix_norm), row(pre_mlp_norm), row(post_mlp_norm))
    received_mlp, _, received_out = received_late

    g_mine, small_gathered = _chip_sum(
        [pair_sums[n] for n in big], list(received_in[:1]) + list(received_out[:1]) + list(received_mlp[:2]),
        exchange=_small_gather(*small))
    g_theirs = _run_exchange(_pair_share(g_mine), "grads_pair_share")

    g_big, d_big, nm_big, nv_big, _ = _adamw_big(
        [weights[n] for n in big[1:]], g_mine[1:], g_theirs[1:], [m_in[n] for n in big[1:]], [v_in[n] for n in big[1:]])
    in_t = _adamw_in(w_in.T, g_mine[0], g_theirs[0], m_w_in.T, v_w_in.T)
    g_small, d_small, nm_small, nv_small, loss_total = _small_adamw(
        small_gathered, [row(weights[n]) for n in SMALL_NAMES], [row(m_in[n]) for n in SMALL_NAMES],
        [row(v_in[n]) for n in SMALL_NAMES])

    grads, delta, new_m, new_v = {}, {}, {}, {}
    grads["w_in"], delta["w_in"], new_m["w_in"], new_v["w_in"] = [t.T for t in in_t]
    for i, n in enumerate(big[1:]):
        grads[n], delta[n], new_m[n], new_v[n] = g_big[i], d_big[i], nm_big[i], nv_big[i]
    for i, n in enumerate(SMALL_NAMES):
        shape = weights[n].shape
        grads[n], delta[n], new_m[n], new_v[n] = (g_small[i].reshape(shape), d_small[i].reshape(shape),
                                                  nm_small[i].reshape(shape), nv_small[i].reshape(shape))
    return (loss_total[0, 0], grad_x[None], *[grads[n] for n in order_w], *[delta[n] for n in order_w], *[new_m[n] for n in order_w],
            *[new_v[n] for n in order_w])
```

```python
import jax
import jax.numpy as jnp
from jax import lax
from jax.experimental import pallas as pl
from jax.experimental.pallas import tpu as pltpu

F32 = jnp.float32
BF16 = jnp.bfloat16
MESH = pl.DeviceIdType.MESH

S = 2048
D = 1024
NFH, FHD = 8, 64
NPAIR = NFH // 2
NGH, GHD = 4, 128
DFOX = NFH * FHD
DGDN = NGH * GHD
CHUNK = 64
NCH = S // CHUNK
CONV_K = 4
DFF = 4 * D
EPS = 1e-6
DPROJ = 3600
LANES = 128
DPROJ_PAD = 3840
BLK_GDN = 12
BLK_GZ = 24
BLK_SMALL = 28
NCHIP = 4
NDEV = 8
VMEM_LIMIT = 56 * 1024 * 1024

ADAM_LR = 0.001
ADAM_B1 = 0.9
ADAM_B2 = 0.999
ADAM_EPS = 1e-08
ADAM_WD = 0.01
ADAM_STEP = 10


def _cparams(**kw):
    return pltpu.CompilerParams(vmem_limit_bytes=VMEM_LIMIT, **kw)


def _dn(ca, cb):
    return (((ca,), (cb,)), ((), ()))


def _dot(a, b, ca=1, cb=0):
    return lax.dot_general(a.astype(BF16), b.astype(BF16), _dn(ca, cb), preferred_element_type=F32)


def _hdot(a, b, ca=1, cb=0):
    return lax.dot_general(a.astype(F32), b.astype(F32), _dn(ca, cb), precision=lax.Precision.HIGHEST,
                           preferred_element_type=F32)


def _dot3(a, b, ca=1, cb=0):
    a_hi, b_hi = a.astype(BF16), b.astype(BF16)
    a_lo, b_lo = (a - a_hi.astype(F32)).astype(BF16), (b - b_hi.astype(F32)).astype(BF16)
    dn = _dn(ca, cb)
    return (lax.dot_general(a_hi, b_hi, dn, preferred_element_type=F32)
            + (lax.dot_general(a_hi, b_lo, dn, preferred_element_type=F32)
               + lax.dot_general(a_lo, b_hi, dn, preferred_element_type=F32)))


@jax.custom_vjp
def _mm_nn(a, b):
    return _dot(a, b, 1, 0)


def _mm_nn_fwd(a, b):
    return _dot(a, b, 1, 0), (a, b)


def _mm_nn_bwd(res, g):
    a, b = res
    return _dot(g, b, 1, 1), _dot(a, g, 0, 0)


_mm_nn.defvjp(_mm_nn_fwd, _mm_nn_bwd)


@jax.custom_vjp
def _mm_nt(a, b):
    return _dot(a, b, 1, 1)


def _mm_nt_fwd(a, b):
    return _dot(a, b, 1, 1), (a, b)


def _mm_nt_bwd(res, g):
    a, b = res
    return _dot(g, b, 1, 0), _dot(g, a, 0, 0)


_mm_nt.defvjp(_mm_nt_fwd, _mm_nt_bwd)


@jax.custom_vjp
def _saved_inverse(m, t_inv):
    del m
    return t_inv


def _saved_inverse_fwd(m, t_inv):
    del m
    return t_inv, t_inv


def _saved_inverse_bwd(t_inv, g):
    return -_dot3(_dot3(t_inv, g, 0, 0), t_inv, 1, 1), jnp.zeros_like(t_inv)


_saved_inverse.defvjp(_saved_inverse_fwd, _saved_inverse_bwd)


def _sigmoid(z):
    return 1.0 / (1.0 + jnp.exp(-z))


def _softplus(z):
    return jnp.maximum(z, 0.0) + jnp.log(1.0 + jnp.exp(-jnp.abs(z)))


def _silu(z):
    return z * _sigmoid(z)


def _rms_scale(x):
    return lax.rsqrt(jnp.mean(x * x, axis=-1, keepdims=True) + EPS)


def _rms_bwd(x, w, g):
    r = _rms_scale(x)
    gw = g * w
    dx = r * gw - x * (r * r * r) * jnp.mean(gw * x, axis=-1, keepdims=True)
    return dx, g * x * r


def _matmul(a, b, *, name, ta=False, tb=False, tm=512, tn=512, tk=512, out_dtypes=(F32,), b3=False, o3=False,
            extra=(), epilogue=None, exchange=None):
    m, k = (a.shape[1], a.shape[0]) if ta else a.shape
    if b3:
        n = b.shape[1] if tb else b.shape[0] * b.shape[2]
        kb = b.shape[0] * b.shape[2] if tb else b.shape[1]
    else:
        n, kb = (b.shape[0], b.shape[1]) if tb else (b.shape[1], b.shape[0])
    assert kb == k, (name, kb, k)
    tm, tn, tk = min(tm, m), min(tn, n), min(tk, k)
    assert m % tm == 0 and n % tn == 0 and k % tk == 0, (name, m, n, k, tm, tn, tk)
    nk = k // tk
    whole_k_blocks = b3 and tb and not ta and nk == 1 and b.shape[0] > 1
    n_extra = len(extra)
    n_out = len(out_dtypes)
    grid = (m // tm, n // tn, nk)
    ex_in, ex_in_specs, ex_out_specs, ex_out_shape, ex_scratch = _hosted(exchange)

    def body(*refs):
        a_ref, b_ref = refs[0], refs[1]
        extra_refs = refs[2:2 + n_extra]
        first_out = 2 + n_extra + len(ex_in)
        out_refs = refs[first_out:first_out + n_out]
        ex_refs = refs[2 + n_extra:first_out] + refs[first_out + n_out:first_out + n_out + len(ex_out_shape)] + refs[-2:]
        step = [pl.program_id(d) for d in range(3)]

        if exchange is not None:
            @pl.when((step[0] == 0) & (step[1] == 0) & (step[2] == 0))
            def _():
                exchange.start(*exchange.split(ex_refs))

        def finish(acc):
            outs = (acc,) if epilogue is None else epilogue(acc, *[r[...] for r in extra_refs])
            for o_ref, val in zip(out_refs, outs):
                o_ref[...] = val.astype(o_ref.dtype)

        if whole_k_blocks:
            width = b.shape[2]
            part = _dot(a_ref[:, 0:width], b_ref[0], 1, 1)
            for blk in range(1, b.shape[0]):
                part = part + _dot(a_ref[:, blk * width:(blk + 1) * width], b_ref[blk], 1, 1)
        else:
            part = _dot(a_ref[...], b_ref[...], 0 if ta else 1, 1 if tb else 0)
        if nk == 1:
            finish(part)
        else:
            acc_ref = refs[first_out + n_out + len(ex_out_shape)]

            @pl.when(step[2] == 0)
            def _():
                acc_ref[...] = part

            @pl.when(step[2] > 0)
            def _():
                acc_ref[...] += part

            @pl.when(step[2] == nk - 1)
            def _():
                finish(acc_ref[...])

        if exchange is not None:
            flat = (step[0] * grid[1] + step[1]) * nk + step[2]
            total = grid[0] * grid[1] * nk

            @pl.when(flat == total // 2)
            def _():
                exchange.middle(*exchange.split(ex_refs))

            @pl.when(flat == total - 1)
            def _():
                exchange.rest(*exchange.split(ex_refs))

    a_spec = pl.BlockSpec((tk, tm), lambda i, j, kk: (kk, i)) if ta else pl.BlockSpec((tm, tk), lambda i, j, kk: (i, kk))
    if whole_k_blocks:
        b_spec = pl.BlockSpec((b.shape[0], tn, b.shape[2]), lambda i, j, kk: (0, j, 0))
    elif b3 and tb:
        assert b.shape[2] == tk
        b_spec = pl.BlockSpec((None, tn, tk), lambda i, j, kk: (kk, j, 0))
    elif b3:
        assert b.shape[2] == tn
        b_spec = pl.BlockSpec((None, tk, tn), lambda i, j, kk: (j, kk, 0))
    elif tb:
        b_spec = pl.BlockSpec((tn, tk), lambda i, j, kk: (j, kk))
    else:
        b_spec = pl.BlockSpec((tk, tn), lambda i, j, kk: (kk, j))
    tile = pl.BlockSpec((tm, tn), lambda i, j, kk: (i, j))
    out_specs = [tile] * n_out
    out_shape = [jax.ShapeDtypeStruct((m, n), dt) for dt in out_dtypes]
    if o3:
        out_specs[0] = pl.BlockSpec((None, tm, tn), lambda i, j, kk: (j, i, 0))
        out_shape[0] = jax.ShapeDtypeStruct((n // tn, m, tn), out_dtypes[0])
    res = pl.pallas_call(
        body, name=name, grid=grid,
        in_specs=[a_spec, b_spec] + [tile if e.shape[0] == m else pl.BlockSpec((1, tn), lambda i, j, kk: (0, j)) for e in extra]
        + ex_in_specs, out_specs=out_specs + ex_out_specs,
        out_shape=out_shape + ex_out_shape,
        scratch_shapes=([pltpu.VMEM((tm, tn), F32)] if nk > 1 else []) + ex_scratch,
        compiler_params=_cparams(),
    )(a, b, *extra, *ex_in)
    if exchange is not None:
        return (res[0] if n_out == 1 else res[:n_out]), res[n_out:]
    return res[0] if n_out == 1 else res


TR = 256


def _row_spec(cols):
    return pl.BlockSpec((TR, cols), lambda i: (i, 0))


def _vec_spec(cols):
    return pl.BlockSpec((1, cols), lambda i: (0, 0))


def _pre_norm(x, w, exchange=None):
    ex_in, ex_in_specs, ex_out_specs, ex_out_shape, ex_scratch = _hosted(exchange)

    def body(*refs):
        x_ref, w_ref, h_ref = refs[0], refs[1], refs[2 + len(ex_in)]
        ex_refs = refs[2:2 + len(ex_in)] + refs[3 + len(ex_in):]
        if exchange is not None:
            @pl.when(pl.program_id(0) == 0)
            def _():
                exchange.start(*exchange.split(ex_refs))

        xv = x_ref[...]
        h_ref[...] = (xv * _rms_scale(xv) * w_ref[...]).astype(BF16)

        if exchange is not None:
            @pl.when(pl.program_id(0) == S // TR - 1)
            def _():
                exchange.finish(*exchange.split(ex_refs))

    res = pl.pallas_call(
        body, name="pre_norm", grid=(S // TR,), in_specs=[_row_spec(D), _vec_spec(D)] + ex_in_specs,
        out_specs=[_row_spec(D)] + ex_out_specs, out_shape=[jax.ShapeDtypeStruct((S, D), BF16)] + ex_out_shape,
        scratch_shapes=ex_scratch, compiler_params=_cparams(),
    )(x, w, *ex_in)
    return res[0], res[1:]


def _loss_head(x1, y, w_post_mlp, target):
    def body(x1_ref, y_ref, w_ref, t_ref, dx2_ref, dy_ref, dw_ref, loss_ref):
        i = pl.program_id(0)
        yv = y_ref[...]
        w = w_ref[...]
        x2 = x1_ref[...] + yv * _rms_scale(yv) * w
        err = x2 - t_ref[...]
        dx2 = err * (1.0 / D)
        dx2_ref[...] = dx2
        dy, dwt = _rms_bwd(yv, w, dx2)
        dy_ref[...] = dy.astype(BF16)

        @pl.when(i == 0)
        def _():
            dw_ref[...] = jnp.zeros_like(dw_ref)
            loss_ref[...] = jnp.zeros_like(loss_ref)

        dw_ref[...] += jnp.sum(dwt, axis=0, keepdims=True)
        part = 0.5 * jnp.sum(jnp.mean(err * err, axis=-1, keepdims=True), axis=0, keepdims=True)
        loss_ref[...] += jnp.broadcast_to(part, loss_ref.shape)

    return pl.pallas_call(
        body, name="loss_head", grid=(S // TR,),
        in_specs=[_row_spec(D), _row_spec(D), _vec_spec(D), _row_spec(D)],
        out_specs=[_row_spec(D), _row_spec(D), _vec_spec(D), _vec_spec(LANES)],
        out_shape=[jax.ShapeDtypeStruct((S, D), F32), jax.ShapeDtypeStruct((S, D), BF16),
                   jax.ShapeDtypeStruct((1, D), F32), jax.ShapeDtypeStruct((1, LANES), F32)],
        compiler_params=_cparams(),
    )(x1, y, w_post_mlp, target)


def _mid_bwd(dh2, x1, w_pre_mlp, dx2, mixed, w_post):
    def body(dh2_ref, x1_ref, wm_ref, dx2_ref, m_ref, wp_ref, dx1_ref, dm_ref, dwm_ref, dwp_ref):
        i = pl.program_id(0)
        dxa, dwm = _rms_bwd(x1_ref[...], wm_ref[...], dh2_ref[...])
        dx1 = dx2_ref[...] + dxa
        dx1_ref[...] = dx1
        dm, dwp = _rms_bwd(m_ref[...], wp_ref[...], dx1)
        dm_ref[...] = dm.astype(BF16)

        @pl.when(i == 0)
        def _():
            dwm_ref[...] = jnp.zeros_like(dwm_ref)
            dwp_ref[...] = jnp.zeros_like(dwp_ref)

        dwm_ref[...] += jnp.sum(dwm, axis=0, keepdims=True)
        dwp_ref[...] += jnp.sum(dwp, axis=0, keepdims=True)

    return pl.pallas_call(
        body, name="mid_bwd", grid=(S // TR,),
        in_specs=[_row_spec(D), _row_spec(D), _vec_spec(D), _row_spec(D), _row_spec(D), _vec_spec(D)],
        out_specs=[_row_spec(D), _row_spec(D), _vec_spec(D), _vec_spec(D)],
        out_shape=[jax.ShapeDtypeStruct((S, D), F32), jax.ShapeDtypeStruct((S, D), BF16),
                   jax.ShapeDtypeStruct((1, D), F32), jax.ShapeDtypeStruct((1, D), F32)],
        compiler_params=_cparams(),
    )(dh2, x1, w_pre_mlp, dx2, mixed, w_post)


def _pre_norm_bwd(dh, x, w, dx1):
    def body(dh_ref, x_ref, w_ref, dx1_ref, dx_ref, dw_ref):
        i = pl.program_id(0)
        dxa, dwt = _rms_bwd(x_ref[...], w_ref[...], dh_ref[...])
        dx_ref[...] = dx1_ref[...] + dxa

        @pl.when(i == 0)
        def _():
            dw_ref[...] = jnp.zeros_like(dw_ref)

        dw_ref[...] += jnp.sum(dwt, axis=0, keepdims=True)

    return pl.pallas_call(
        body, name="pre_norm_bwd", grid=(S // TR,),
        in_specs=[_row_spec(D), _row_spec(D), _vec_spec(D), _row_spec(D)], out_specs=[_row_spec(D), _vec_spec(D)],
        out_shape=[jax.ShapeDtypeStruct((S, D), F32), jax.ShapeDtypeStruct((1, D), F32)], compiler_params=_cparams(),
    )(dh, x, w, dx1)


BQ = 512
NQ = S // BQ
LANE_BETA, LANE_G = 8, 12


def _gate_lanes(shape):
    lane = lax.broadcasted_iota(jnp.int32, shape, 1)
    return lane < LANE_BETA, (lane >= LANE_BETA) & (lane < LANE_G), (lane >= LANE_G) & (lane < LANE_G + NGH)


def _gates(proj, bias_vec, alog_vec):
    def body(s_ref, b_ref, a_ref, o_ref, carry_ref):
        i = pl.program_id(0)

        @pl.when(i == 0)
        def _():
            carry_ref[...] = jnp.zeros_like(carry_ref)

        z = s_ref[...] + b_ref[...]
        tail = jnp.log(1.0 + jnp.exp(-jnp.abs(z)))
        sp = jnp.maximum(z, 0.0) + tail
        lf = jnp.minimum(z, 0.0) - tail
        r = lax.broadcasted_iota(jnp.int32, (BQ, BQ), 0)
        c = lax.broadcasted_iota(jnp.int32, (BQ, BQ), 1)
        tri = (c <= r).astype(F32)
        cum = _hdot(tri, lf) + carry_ref[...]
        carry_ref[...] = cum[BQ - 1:BQ, :]
        is_fox, is_beta, is_g = _gate_lanes(z.shape)
        o_ref[...] = jnp.where(is_fox, cum, jnp.where(is_beta, _sigmoid(z), jnp.where(is_g, -jnp.exp(a_ref[...]) * sp, 0.0)))

    return pl.pallas_call(
        body, name="gates", grid=(NQ,),
        in_specs=[pl.BlockSpec((BQ, LANES), lambda i: (i, BLK_SMALL)), _vec_spec(LANES), _vec_spec(LANES)],
        out_specs=pl.BlockSpec((BQ, LANES), lambda i: (i, 0)), out_shape=jax.ShapeDtypeStruct((S, LANES), F32),
        scratch_shapes=[pltpu.VMEM((1, LANES), F32)], compiler_params=_cparams(),
    )(proj, bias_vec, alog_vec)


def _gates_bwd(proj, bias_vec, alog_vec, dgates_gdn, dcum_fox, dproj):
    def body(s_ref, b_ref, a_ref, dg_ref, dc_ref, dproj_in, dproj_ref, red_ref, carry_ref):
        del dproj_in
        i = pl.program_id(0)

        @pl.when(i == 0)
        def _():
            carry_ref[...] = jnp.zeros_like(carry_ref)
            red_ref[...] = jnp.zeros_like(red_ref)

        z = s_ref[...] + b_ref[...]
        dg = dg_ref[...] + dc_ref[...]
        r = lax.broadcasted_iota(jnp.int32, (BQ, BQ), 0)
        c = lax.broadcasted_iota(jnp.int32, (BQ, BQ), 1)
        upper = (c >= r).astype(F32)
        dlf = _hdot(upper, dg) + carry_ref[...]
        carry_ref[...] = dlf[0:1, :]
        sig = _sigmoid(z)
        g_scale = -jnp.exp(a_ref[...])
        is_fox, is_beta, is_g = _gate_lanes(z.shape)
        ds = jnp.where(is_fox, dlf * (1.0 - sig), jnp.where(is_beta, dg * sig * (1.0 - sig), jnp.where(is_g, dg * g_scale * sig, 0.0)))
        dproj_ref[:, 0:LANES] = ds.astype(BF16)
        dproj_ref[:, LANES:2 * LANES] = jnp.zeros((BQ, LANES), BF16)
        dalog = jnp.where(is_g, dg * g_scale * _softplus(z), 0.0)
        sums = jnp.sum(ds, axis=0, keepdims=True)
        red_ref[0:1, :] += jnp.where(is_fox[0:1], sums, 0.0)
        red_ref[1:2, :] += pltpu.roll(jnp.where(is_g[0:1], sums, 0.0), LANES - LANE_G, 1)
        red_ref[2:3, :] += pltpu.roll(jnp.sum(dalog, axis=0, keepdims=True), LANES - LANE_G, 1)

    blk = pl.BlockSpec((BQ, LANES), lambda i: (NQ - 1 - i, 0))
    return pl.pallas_call(
        body, name="gates_bwd", grid=(NQ,),
        in_specs=[pl.BlockSpec((BQ, LANES), lambda i: (NQ - 1 - i, BLK_SMALL)), _vec_spec(LANES), _vec_spec(LANES), blk, blk,
                  pl.BlockSpec(memory_space=pl.ANY)],
        out_specs=[pl.BlockSpec((BQ, 2 * LANES), lambda i: (NQ - 1 - i, BLK_SMALL // 2)), pl.BlockSpec((8, LANES), lambda i: (0, 0))],
        out_shape=[jax.ShapeDtypeStruct((S, DPROJ_PAD), BF16), jax.ShapeDtypeStruct((8, LANES), F32)],
        input_output_aliases={5: 0},
        scratch_shapes=[pltpu.VMEM((1, LANES), F32)], compiler_params=_cparams(),
    )(proj, bias_vec, alog_vec, dgates_gdn, dcum_fox, dproj)


FOX_SCALE = FHD ** -0.5
FOX_PAIRS = 2
FOX_PAIRS_BWD = 2


def _head_mask(e):
    lane = lax.broadcasted_iota(jnp.int32, (1, LANES), 1)
    return (lane >= e * FHD) & (lane < (e + 1) * FHD)


def _lane_col(vals, index):
    lane = lax.broadcasted_iota(jnp.int32, vals.shape, 1)
    return jnp.sum(jnp.where(lane == index, vals, 0.0), axis=1, keepdims=True)


def _sublane_row(vals, index):
    row = lax.broadcasted_iota(jnp.int32, vals.shape, 0)
    return jnp.sum(jnp.where(row == index, vals, 0.0), axis=0, keepdims=True)


def _pair_cols(c0, c1):
    lane = lax.broadcasted_iota(jnp.int32, (c0.shape[0], 2), 1)
    return jnp.where(lane == 0, c0, c1)


def _split3(x):
    hi = x.astype(BF16).astype(F32)
    rest = x - hi
    mid = rest.astype(BF16).astype(F32)
    return hi, mid, (rest - mid).astype(BF16).astype(F32)


def _fox_operand(vals, e, cum, is_query):
    lane = lax.broadcasted_iota(jnp.int32, (1, LANES), 1)
    base = (1 - e) * FHD
    parts = _split3(cum)
    own = jnp.where(_head_mask(e), vals * FOX_SCALE if is_query else vals, 0.0)
    cum_at, ones_at = (base, base + 3) if is_query else (base + 3, base)
    sign = 1.0 if is_query else -1.0
    out = own + jnp.where((lane >= ones_at) & (lane < ones_at + 3), 1.0, 0.0)
    for i, part in enumerate(parts):
        out = out + jnp.where(lane == cum_at + i, sign * part, 0.0)
    return out.astype(BF16)


def _causal_block():
    return lax.broadcasted_iota(jnp.int32, (BQ, BQ), 1) <= lax.broadcasted_iota(jnp.int32, (BQ, BQ), 0)


def _head_rms(o, masks):
    o2 = o * o
    r = [lax.rsqrt(jnp.sum(jnp.where(mk, o2, 0.0), axis=1, keepdims=True) * (1.0 / FHD) + EPS) for mk in masks]
    return jnp.where(masks[0], r[0], r[1])


def _hosted(exchange):
    if exchange is None:
        return [], [], [], [], []
    return (exchange.inputs, [HBM] * len(exchange.inputs), [HBM] * len(exchange.out_shape), exchange.out_shape,
            exchange.sem_shapes())


def _fox_fwd(proj, gates, w2, exchange=None):
    ex_in, ex_in_specs, ex_out_specs, ex_out_shape, ex_scratch = _hosted(exchange)

    n_in = 3 * FOX_PAIRS + 2
    heads = [(pp, e) for pp in range(FOX_PAIRS) for e in range(2)]

    def body(*refs):
        qkv_refs, g_ref, w_ref = refs[:3 * FOX_PAIRS], refs[3 * FOX_PAIRS], refs[3 * FOX_PAIRS + 1]
        mix_ref, o_ref, lse_ref = refs[n_in + len(ex_in):n_in + 3 + len(ex_in)]
        ka_ref, vb_ref = refs[n_in + 3 + len(ex_in) + len(ex_out_shape):n_in + 5 + len(ex_in) + len(ex_out_shape)]
        ex_refs = refs[n_in:n_in + len(ex_in)] + refs[n_in + 3 + len(ex_in):n_in + 3 + len(ex_in) + len(ex_out_shape)] + refs[-2:]
        grp, qi = pl.program_id(0), pl.program_id(1)

        def head_index(pp, e):
            return 2 * (FOX_PAIRS * grp + pp) + e

        if exchange is not None:
            @pl.when((grp == 0) & (qi == 0))
            def _():
                exchange.start(*exchange.split(ex_refs))

        @pl.when(qi == 0)
        def _():
            gt = g_ref[...]
            for pp in range(FOX_PAIRS):
                kv = qkv_refs[3 * pp + 1][...]
                for e in range(2):
                    ka_ref[2 * pp + e] = _fox_operand(kv, e, _lane_col(gt, head_index(pp, e)), False)
                vb_ref[pp] = qkv_refs[3 * pp + 2][...].astype(BF16)

        masks = [_head_mask(0), _head_mask(1)]
        gt = g_ref[pl.ds(pl.multiple_of(qi * BQ, BQ), BQ), :]
        qs = [_fox_operand(qkv_refs[3 * pp][...], e, _lane_col(gt, head_index(pp, e)), True) for pp, e in heads]
        n = range(len(heads))

        def block(kj, carry, diagonal):
            rows = pl.ds(pl.multiple_of(kj * BQ, BQ), BQ)
            s = [_dot(qs[i], ka_ref[i, rows, :], 1, 1) for i in n]
            if diagonal:
                s = [jnp.where(_causal_block(), s[i], -jnp.inf) for i in n]
            m_new = [jnp.maximum(carry[i][0], jnp.max(s[i], axis=-1, keepdims=True)) for i in n]
            p = [jnp.exp(s[i] - m_new[i]) for i in n]
            alpha = [jnp.exp(carry[i][0] - m_new[i]) for i in n]
            l_new = [alpha[i] * carry[i][1] + jnp.sum(p[i], axis=-1, keepdims=True) for i in n]
            pv = [_dot(p[i], vb_ref[heads[i][0], rows, :]) for i in n]
            return tuple((m_new[i], l_new[i], alpha[i] * carry[i][2] + pv[i]) for i in n)

        one = (jnp.full((BQ, 1), -jnp.inf, F32), jnp.zeros((BQ, 1), F32), jnp.zeros((BQ, LANES), F32))
        below = lax.fori_loop(0, qi, lambda kj, carry: block(kj, carry, False), (one,) * len(heads))
        done = block(qi, below, True)
        for pp in range(FOX_PAIRS):
            (m0, l0, a0), (m1, l1, a1) = done[2 * pp], done[2 * pp + 1]
            o = jnp.where(masks[0], a0 / l0, a1 / l1)
            cols = slice(pp * LANES, (pp + 1) * LANES)
            o_ref[:, cols] = o
            mix_ref[:, cols] = (o * _head_rms(o, masks) * w_ref[...]).astype(BF16)
            lse_ref[pp] = _pair_cols(m0 + jnp.log(l0), m1 + jnp.log(l1))

        if exchange is not None:
            @pl.when((grp == NPAIR // FOX_PAIRS // 2) & (qi == 0))
            def _():
                exchange.middle(*exchange.split(ex_refs))

            @pl.when((grp == NPAIR // FOX_PAIRS - 1) & (qi == NQ - 1))
            def _():
                exchange.rest(*exchange.split(ex_refs))

    qkv_specs = []
    for pp in range(FOX_PAIRS):
        qkv_specs.append(pl.BlockSpec((BQ, LANES), lambda g, i, pp=pp: (i, 3 * (FOX_PAIRS * g + pp))))
        qkv_specs.append(pl.BlockSpec((S, LANES), lambda g, i, pp=pp: (0, 3 * (FOX_PAIRS * g + pp) + 1)))
        qkv_specs.append(pl.BlockSpec((S, LANES), lambda g, i, pp=pp: (0, 3 * (FOX_PAIRS * g + pp) + 2)))
    blk = pl.BlockSpec((BQ, FOX_PAIRS * LANES), lambda g, i: (i, g))
    res = pl.pallas_call(
        body, name="fox_fwd", grid=(NPAIR // FOX_PAIRS, NQ),
        in_specs=qkv_specs + [pl.BlockSpec((S, LANES), lambda g, i: (0, 0)), pl.BlockSpec((1, LANES), lambda g, i: (0, 0))]
        + ex_in_specs,
        out_specs=[blk, blk, pl.BlockSpec((FOX_PAIRS, BQ, 2), lambda g, i: (g, i, 0))] + ex_out_specs,
        out_shape=[jax.ShapeDtypeStruct((S, D), BF16), jax.ShapeDtypeStruct((S, DFOX), F32),
                   jax.ShapeDtypeStruct((NPAIR, S, 2), F32)] + ex_out_shape,
        scratch_shapes=[pltpu.VMEM((2 * FOX_PAIRS, S, LANES), BF16), pltpu.VMEM((FOX_PAIRS, S, LANES), BF16)] + ex_scratch,
        compiler_params=_cparams(),
    )(*([proj] * (3 * FOX_PAIRS)), gates, w2, *ex_in)
    return res[0], res[1], res[2], res[3:]


def _fox_norm_bwd(o, dmix, w2, exchange=None):
    ex_in, ex_in_specs, ex_out_specs, ex_out_shape, ex_scratch = _hosted(exchange)

    def body(*refs):
        o_ref, g_ref, w_ref = refs[:3]
        do_ref, dl_ref, dw_ref = refs[3 + len(ex_in):6 + len(ex_in)]
        ex_refs = refs[3:3 + len(ex_in)] + refs[6 + len(ex_in):]
        hp, qi = pl.program_id(0), pl.program_id(1)

        if exchange is not None:
            @pl.when((hp == 0) & (qi == 0))
            def _():
                exchange.start(*exchange.split(ex_refs))

        masks = [_head_mask(0), _head_mask(1)]
        ov = o_ref[...]
        g = g_ref[...]
        r = _head_rms(ov, masks)
        gw = g * w_ref[...]
        gwo = gw * ov
        mean = [jnp.sum(jnp.where(mk, gwo, 0.0), axis=1, keepdims=True) * (1.0 / FHD) for mk in masks]
        do = r * gw - ov * (r * r * r) * jnp.where(masks[0], mean[0], mean[1])
        do_ref[...] = do.astype(BF16)
        doo = do * ov
        dl_ref[...] = _pair_cols(*[jnp.sum(jnp.where(mk, doo, 0.0), axis=1, keepdims=True) for mk in masks])

        @pl.when((hp == 0) & (qi == 0))
        def _():
            dw_ref[...] = jnp.zeros_like(dw_ref)

        dw_ref[...] += jnp.sum(g * ov * r, axis=0, keepdims=True)

        @pl.when((hp == NPAIR - 1) & (qi == NQ - 1))
        def _():
            dw = dw_ref[...]
            dw_ref[...] = dw + pltpu.roll(dw, FHD, 1)
            if exchange is not None:
                exchange.finish(*exchange.split(ex_refs))

    blk = pl.BlockSpec((BQ, LANES), lambda hp, i: (i, hp))
    vec = pl.BlockSpec((1, LANES), lambda hp, i: (0, 0))
    res = pl.pallas_call(
        body, name="fox_norm_bwd", grid=(NPAIR, NQ), in_specs=[blk, blk, vec] + ex_in_specs,
        out_specs=[blk, pl.BlockSpec((None, BQ, 2), lambda hp, i: (hp, i, 0)), vec] + ex_out_specs,
        out_shape=[jax.ShapeDtypeStruct((S, DFOX), BF16), jax.ShapeDtypeStruct((NPAIR, S, 2), F32),
                   jax.ShapeDtypeStruct((1, LANES), F32)] + ex_out_shape,
        scratch_shapes=ex_scratch, compiler_params=_cparams(),
    )(o, dmix, w2, *ex_in)
    return res[0], res[1], res[2], res[3:]


def _fox_bwd(proj, do, gates, lse, delta, exchange=None):
    ex_in, ex_in_specs, ex_out_specs, ex_out_shape, ex_scratch = _hosted(exchange)

    pg = FOX_PAIRS_BWD
    n_in = 3 * pg + 4
    heads = [(pp, e) for pp in range(pg) for e in range(2)]

    def body(*refs):
        qkv_refs = refs[:3 * pg]
        do_ref, g_ref, lse_ref, dl_ref = refs[3 * pg:n_in]
        dproj_ref, dc_ref = refs[n_in + len(ex_in):n_in + 2 + len(ex_in)]
        qa_ref, dq_ref = refs[n_in + 2 + len(ex_in) + len(ex_out_shape):n_in + 4 + len(ex_in) + len(ex_out_shape)]
        ex_refs = refs[n_in:n_in + len(ex_in)] + refs[n_in + 2 + len(ex_in):n_in + 2 + len(ex_in) + len(ex_out_shape)] + refs[-2:]
        grp, kj = pl.program_id(0), pl.program_id(1)

        def head_index(pp, e):
            return 2 * (pg * grp + pp) + e

        if exchange is not None:
            @pl.when((grp == 0) & (kj == 0))
            def _():
                exchange.start(*exchange.split(ex_refs))

        @pl.when(kj == 0)
        def _():
            gt = g_ref[...]
            for pp in range(pg):
                qv = qkv_refs[3 * pp][...]
                for e in range(2):
                    qa_ref[2 * pp + e] = _fox_operand(qv, e, _lane_col(gt, head_index(pp, e)), True)
            dq_ref[...] = jnp.zeros_like(dq_ref)

        @pl.when((grp == 0) & (kj == 0))
        def _():
            dc_ref[...] = jnp.zeros_like(dc_ref)

        masks = [_head_mask(0), _head_mask(1)]
        krows = pl.ds(pl.multiple_of(kj * BQ, BQ), BQ)
        gk = g_ref[krows, :]
        kas = [_fox_operand(qkv_refs[3 * pp + 1][...], e, _lane_col(gk, head_index(pp, e)), False) for pp, e in heads]
        vbs = [qkv_refs[3 * pp + 2][...].astype(BF16) for pp in range(pg)]
        lane = lax.broadcasted_iota(jnp.int32, (BQ, LANES), 1)
        n = range(len(heads))

        def block(qi, carry, diagonal):
            dks, dvs, css = carry
            rows = pl.ds(pl.multiple_of(qi * BQ, BQ), BQ)
            qa = [qa_ref[i, rows, :] for i in n]
            s = [_dot(qa[i], kas[i], 1, 1) for i in n]
            if diagonal:
                s = [jnp.where(_causal_block(), s[i], -jnp.inf) for i in n]
            dov = [do_ref[rows, pp * LANES:(pp + 1) * LANES] for pp in range(pg)]
            doe = [jnp.where(masks[e], dov[pp], jnp.zeros_like(dov[pp])) for pp, e in heads]
            lse2 = [lse_ref[pp, rows, :] for pp in range(pg)]
            dl2 = [dl_ref[pp, rows, :] for pp in range(pg)]
            p = [jnp.exp(s[i] - _lane_col(lse2[heads[i][0]], heads[i][1])) for i in n]
            dp = [_dot(doe[i], vbs[heads[i][0]], 1, 1) for i in n]
            ds = [p[i] * (dp[i] - _lane_col(dl2[heads[i][0]], heads[i][1])) for i in n]
            dv_part = [_dot(p[i], doe[i], 0, 0) for i in n]
            dk_part = [_dot(ds[i], jnp.where(masks[heads[i][1]], qa[i], jnp.zeros_like(qa[i])), 0, 0) for i in n]
            dq_part = [jnp.where(masks[heads[i][1]], _dot(ds[i], kas[i]), 0.0) for i in n]
            css = tuple(css[i] + jnp.sum(ds[i], axis=0, keepdims=True) for i in n)
            dc = jnp.zeros((BQ, LANES), F32)
            for i in n:
                dc = dc + jnp.where(lane == head_index(*heads[i]), jnp.sum(ds[i], axis=1, keepdims=True), 0.0)
            for pp in range(pg):
                dq_ref[pp, rows, :] += (dq_part[2 * pp] + dq_part[2 * pp + 1]) * FOX_SCALE
            dc_ref[rows, :] += dc
            dks = tuple(dks[pp] + dk_part[2 * pp] + dk_part[2 * pp + 1] for pp in range(pg))
            dvs = tuple(dvs[pp] + dv_part[2 * pp] + dv_part[2 * pp + 1] for pp in range(pg))
            return dks, dvs, css

        zero = jnp.zeros((BQ, LANES), F32)
        first = block(kj, ((zero,) * pg, (zero,) * pg, (jnp.zeros((1, BQ), F32),) * len(heads)), True)
        dks, dvs, css = lax.fori_loop(kj + 1, NQ, lambda qi, carry: block(qi, carry, False), first)
        r = lax.broadcasted_iota(jnp.int32, (BQ, BQ), 0)
        c = lax.broadcasted_iota(jnp.int32, (BQ, BQ), 1)
        dcol = jnp.zeros((BQ, LANES), F32)
        for i in n:
            col = jnp.sum(jnp.where(r == c, css[i], 0.0), axis=1, keepdims=True)
            dcol = dcol + jnp.where(lane == head_index(*heads[i]), col, 0.0)
        dc_ref[krows, :] -= dcol
        for pp in range(pg):
            base = 3 * pp * LANES
            dproj_ref[krows, base + LANES:base + 2 * LANES] = dks[pp].astype(BF16)
            dproj_ref[krows, base + 2 * LANES:base + 3 * LANES] = dvs[pp].astype(BF16)

        @pl.when(kj == NQ - 1)
        def _():
            for pp in range(pg):
                dproj_ref[:, 3 * pp * LANES:(3 * pp + 1) * LANES] = dq_ref[pp].astype(BF16)

        if exchange is not None:
            @pl.when((grp == NPAIR // pg // 2) & (kj == 0))
            def _():
                exchange.middle(*exchange.split(ex_refs))

            @pl.when((grp == NPAIR // pg - 1) & (kj == NQ - 1))
            def _():
                exchange.rest(*exchange.split(ex_refs))

    qkv_specs = []
    for pp in range(pg):
        qkv_specs.append(pl.BlockSpec((S, LANES), lambda g, j, pp=pp: (0, 3 * (pg * g + pp))))
        qkv_specs.append(pl.BlockSpec((BQ, LANES), lambda g, j, pp=pp: (j, 3 * (pg * g + pp) + 1)))
        qkv_specs.append(pl.BlockSpec((BQ, LANES), lambda g, j, pp=pp: (j, 3 * (pg * g + pp) + 2)))
    pair = pl.BlockSpec((pg, S, 2), lambda g, j: (g, 0, 0))
    res = pl.pallas_call(
        body, name="fox_bwd", grid=(NPAIR // pg, NQ),
        in_specs=qkv_specs + [pl.BlockSpec((S, pg * LANES), lambda g, j: (0, g)), pl.BlockSpec((S, LANES), lambda g, j: (0, 0)),
                              pair, pair] + ex_in_specs,
        out_specs=[pl.BlockSpec((S, 3 * pg * LANES), lambda g, j: (0, g)), pl.BlockSpec((S, LANES), lambda g, j: (0, 0))]
        + ex_out_specs,
        out_shape=[jax.ShapeDtypeStruct((S, DPROJ_PAD), BF16), jax.ShapeDtypeStruct((S, LANES), F32)] + ex_out_shape,
        scratch_shapes=[pltpu.VMEM((2 * pg, S, LANES), BF16), pltpu.VMEM((pg, S, LANES), F32)] + ex_scratch,
        compiler_params=_cparams(),
    )(*([proj] * (3 * pg)), do, gates, lse, delta, *ex_in)
    return res[0], res[1], res[2:]


NQKV = 3 * NGH
GDN_QSCALE = GHD ** -0.5


def _shift_down(x, s):
    if s == 0:
        return x
    row = lax.broadcasted_iota(jnp.int32, x.shape, 0)
    return jnp.where(row >= s, pltpu.roll(x, s, 0), 0.0)


def _shift_up(x, s):
    if s == 0:
        return x
    n = x.shape[0]
    row = lax.broadcasted_iota(jnp.int32, x.shape, 0)
    return jnp.where(row < n - s, pltpu.roll(x, n - s, 0), 0.0)


def _conv_taps(xv):
    return [_shift_down(xv, CONV_K - 1 - j) for j in range(CONV_K)]


def _conv_pre(taps, wv):
    pre = taps[CONV_K - 1] * wv[CONV_K - 1:CONV_K, :]
    for j in range(CONV_K - 1):
        pre = pre + taps[j] * wv[j:j + 1, :]
    return pre


def _l2_factors(b):
    return b < 2 * NGH, jnp.where(b < NGH, GDN_QSCALE, 1.0)


def _gdn_pre(proj, conv_w):
    def body(x_ref, w_ref, o_ref):
        b = pl.program_id(0)
        c = _silu(_conv_pre(_conv_taps(x_ref[...]), w_ref[...]))
        normed, scale = _l2_factors(b)
        rs = lax.rsqrt(jnp.sum(c * c, axis=-1, keepdims=True) + EPS)
        o_ref[...] = c * jnp.where(normed, rs, 1.0) * scale

    return pl.pallas_call(
        body, name="gdn_pre", grid=(NQKV,),
        in_specs=[pl.BlockSpec((S, GHD), lambda b: (0, BLK_GDN + b)), pl.BlockSpec((CONV_K, GHD), lambda b: (0, b))],
        out_specs=pl.BlockSpec((S, GHD), lambda b: (0, b)),
        out_shape=jax.ShapeDtypeStruct((S, NQKV * GHD), F32), compiler_params=_cparams(),
    )(proj, conv_w)


def _gdn_pre_bwd(proj, conv_w, dqkv, dproj):
    def body(x_ref, w_ref, dy_ref, dproj_in, dx_ref, dw_ref):
        del dproj_in
        b = pl.program_id(0)
        taps = _conv_taps(x_ref[...])
        wv = w_ref[...]
        pre = _conv_pre(taps, wv)
        sig = _sigmoid(pre)
        c = pre * sig
        normed, scale = _l2_factors(b)
        g = dy_ref[...] * scale
        rs = lax.rsqrt(jnp.sum(c * c, axis=-1, keepdims=True) + EPS)
        dc_n = rs * g - c * (rs * rs * rs) * jnp.sum(g * c, axis=-1, keepdims=True)
        dc = jnp.where(normed, dc_n, g)
        dpre = dc * sig * (1.0 + pre * (1.0 - sig))
        dx = dpre * wv[CONV_K - 1:CONV_K, :]
        for j in range(CONV_K - 1):
            dx = dx + _shift_up(dpre, CONV_K - 1 - j) * wv[j:j + 1, :]
        dx_ref[...] = dx.astype(BF16)
        for j in range(CONV_K):
            dw_ref[j:j + 1, :] = jnp.sum(dpre * taps[j], axis=0, keepdims=True)

    return pl.pallas_call(
        body, name="gdn_pre_bwd", grid=(NQKV,),
        in_specs=[pl.BlockSpec((S, GHD), lambda b: (0, BLK_GDN + b)), pl.BlockSpec((CONV_K, GHD), lambda b: (0, b)),
                  pl.BlockSpec((None, S, GHD), lambda b: (b // NGH, 0, b % NGH)), pl.BlockSpec(memory_space=pl.ANY)],
        out_specs=[pl.BlockSpec((S, GHD), lambda b: (0, BLK_GDN + b)), pl.BlockSpec((CONV_K, GHD), lambda b: (0, b))],
        out_shape=[jax.ShapeDtypeStruct((S, DPROJ_PAD), BF16), jax.ShapeDtypeStruct((CONV_K, NQKV * GHD), F32)],
        input_output_aliases={3: 0}, compiler_params=_cparams(),
    )(proj, conv_w, dqkv, dproj)


CB = 16
NCB = NCH // CB


def _chunk_prep(qs, ks, vs, gcols, bcols, t_saved=None):
    n = range(len(qs))
    r = lax.broadcasted_iota(jnp.int32, (CHUNK, CHUNK), 0)
    c = lax.broadcasted_iota(jnp.int32, (CHUNK, CHUNK), 1)
    incl = c <= r
    eye = (r == c).astype(F32)
    grow = [jnp.sum(gcols[i] * eye, axis=0, keepdims=True) for i in n]
    gc_col = [jnp.sum(jnp.where(incl, grow[i], 0.0), axis=1, keepdims=True) for i in n]
    gc_row = [jnp.sum(jnp.where(r <= c, gcols[i], 0.0), axis=0, keepdims=True) for i in n]
    decay = [jnp.exp(jnp.where(incl, gc_col[i] - gc_row[i], -jnp.inf)) for i in n]
    kb = [ks[i] * bcols[i] for i in n]
    vb = [vs[i] * bcols[i] for i in n]
    kk = [_mm_nt(kb[i], ks[i]) for i in n]
    m = [jnp.where(c < r, kk[i] * decay[i], 0.0) for i in n]
    if t_saved is None:
        t_inv = [eye - m[i] for i in n]
        p = [_dot3(m[i], m[i]) for i in n]
        for step in range(5):
            t_inv = [t_inv[i] + _dot3(t_inv[i], p[i]) for i in n]
            if step < 4:
                p = [_dot3(p[i], p[i]) for i in n]
    else:
        t_inv = [_saved_inverse(m[i], t_saved[i]) for i in n]
    egc = [jnp.exp(gc_col[i]) for i in n]
    u = [_mm_nn(t_inv[i], vb[i]) for i in n]
    w = [_mm_nn(t_inv[i], kb[i] * egc[i]) for i in n]
    qk = [_mm_nt(qs[i], ks[i]) for i in n]
    gc_last = [gc_col[i][CHUNK - 1:CHUNK, :] for i in n]
    return [(u[i], w[i], qk[i] * decay[i], qs[i] * egc[i], ks[i] * jnp.exp(gc_last[i] - gc_col[i]), jnp.exp(gc_last[i]),
             t_inv[i]) for i in n]


def _prep_specs():
    rows = CB * CHUNK
    qs = pl.BlockSpec((rows, GHD), lambda i, h: (i, h))
    ks = pl.BlockSpec((rows, GHD), lambda i, h: (i, NGH + h))
    vs = pl.BlockSpec((rows, GHD), lambda i, h: (i, 2 * NGH + h))
    gs = pl.BlockSpec((rows, LANES), lambda i, h: (i, 0))
    a_s = pl.BlockSpec((None, rows, CHUNK), lambda i, h: (h, i, 0))
    gl_s = pl.BlockSpec((None, CB, 1, LANES), lambda i, h: (h, i, 0, 0))
    return qs, ks, vs, gs, a_s, gl_s


def _gdn_prep(qkv, gates, exchange=None):
    ex_in, ex_in_specs, ex_out_specs, ex_out_shape, ex_scratch = _hosted(exchange)

    def body(*refs):
        q_ref, k_ref, v_ref, g_ref = refs[:4]
        u_ref, w_ref, qd_ref, kd_ref, a_ref, gl_ref, t_ref = refs[4 + len(ex_in):11 + len(ex_in)]
        ex_refs = refs[4:4 + len(ex_in)] + refs[11 + len(ex_in):]
        h = pl.program_id(1)

        if exchange is not None:
            @pl.when((pl.program_id(0) == 0) & (h == 0))
            def _():
                exchange.start(*exchange.split(ex_refs))

        chunks = [pl.ds(cidx * CHUNK, CHUNK) for cidx in range(CB)]
        gts = [g_ref[rows, :] for rows in chunks]
        outs = _chunk_prep([q_ref[rows, :] for rows in chunks], [k_ref[rows, :] for rows in chunks],
                           [v_ref[rows, :] for rows in chunks], [_lane_col(gt, LANE_G + h) for gt in gts],
                           [_lane_col(gt, LANE_BETA + h) for gt in gts])
        for cidx, rows in enumerate(chunks):
            u, w, a, qd, kd, gl, t_inv = outs[cidx]
            u_ref[rows, :] = u
            w_ref[rows, :] = w
            qd_ref[rows, :] = qd
            kd_ref[rows, :] = kd
            a_ref[rows, :] = a
            t_ref[rows, :] = t_inv
            gl_ref[cidx] = jnp.broadcast_to(gl, (1, LANES))

        if exchange is not None:
            @pl.when((pl.program_id(0) == NCB // 2) & (h == 0))
            def _():
                exchange.middle(*exchange.split(ex_refs))

            @pl.when((pl.program_id(0) == NCB - 1) & (h == NGH - 1))
            def _():
                exchange.rest(*exchange.split(ex_refs))

    qs, ks, vs, gs, a_s, gl_s = _prep_specs()
    tok = jax.ShapeDtypeStruct((S, DGDN), F32)
    sq = jax.ShapeDtypeStruct((NGH, S, CHUNK), F32)
    res = pl.pallas_call(
        body, name="gdn_prep", grid=(NCB, NGH), in_specs=[qs, ks, vs, gs] + ex_in_specs,
        out_specs=[qs, qs, qs, qs, a_s, gl_s, a_s] + ex_out_specs,
        out_shape=[tok, tok, tok, tok, sq, jax.ShapeDtypeStruct((NGH, NCH, 1, LANES), F32), sq] + ex_out_shape,
        scratch_shapes=ex_scratch, compiler_params=_cparams(),
    )(qkv, qkv, qkv, gates, *ex_in)
    return res[:7], res[7:]


def _gdn_prep_bwd(qkv, gates, t_inv, du, dw, dqd, dkd, da, dgl, exchange=None):
    ex_in, ex_in_specs, ex_out_specs, ex_out_shape, ex_scratch = _hosted(exchange)

    def body(*refs):
        q_ref, k_ref, v_ref, g_ref, t_ref, du_ref, dw_ref, dqd_ref, dkd_ref, da_ref, dgl_ref = refs[:11]
        dqkv_ref, dg_ref = refs[11 + len(ex_in):13 + len(ex_in)]
        ex_refs = refs[11:11 + len(ex_in)] + refs[13 + len(ex_in):]
        h = pl.program_id(1)

        if exchange is not None:
            @pl.when((pl.program_id(0) == 0) & (h == 0))
            def _():
                exchange.start(*exchange.split(ex_refs))

        @pl.when(h == 0)
        def _():
            dg_ref[...] = jnp.zeros_like(dg_ref)

        lane = lax.broadcasted_iota(jnp.int32, (CHUNK, LANES), 1)
        chunks = [pl.ds(cidx * CHUNK, CHUNK) for cidx in range(CB)]
        gts = [g_ref[rows, :] for rows in chunks]
        t_saved = [t_ref[rows, :] for rows in chunks]
        _, vjp = jax.vjp(lambda *args: [o[:6] for o in _chunk_prep(*args, t_saved=t_saved)],
                         [q_ref[rows, :] for rows in chunks], [k_ref[rows, :] for rows in chunks],
                         [v_ref[rows, :] for rows in chunks], [_lane_col(gt, LANE_G + h) for gt in gts],
                         [_lane_col(gt, LANE_BETA + h) for gt in gts])
        dqs, dks, dvs, dgcs, dbcs = vjp([(du_ref[rows, :], dw_ref[rows, :], da_ref[rows, :], dqd_ref[rows, :],
                                          dkd_ref[rows, :], dgl_ref[cidx][:, 0:1]) for cidx, rows in enumerate(chunks)])
        for cidx, rows in enumerate(chunks):
            dq, dk, dv, dgc, dbc = dqs[cidx], dks[cidx], dvs[cidx], dgcs[cidx], dbcs[cidx]
            dqkv_ref[0, rows, :] = dq
            dqkv_ref[1, rows, :] = dk
            dqkv_ref[2, rows, :] = dv
            dg_ref[rows, :] += jnp.where(lane == LANE_G + h, dgc, 0.0) + jnp.where(lane == LANE_BETA + h, dbc, 0.0)

        if exchange is not None:
            @pl.when((pl.program_id(0) == NCB - 1) & (h == NGH - 1))
            def _():
                exchange.finish(*exchange.split(ex_refs))

    qs, ks, vs, gs, a_s, gl_s = _prep_specs()
    res = pl.pallas_call(
        body, name="gdn_prep_bwd", grid=(NCB, NGH), in_specs=[qs, ks, vs, gs, a_s, qs, qs, qs, qs, a_s, gl_s] + ex_in_specs,
        out_specs=[pl.BlockSpec((3, CB * CHUNK, GHD), lambda i, h: (0, i, h)), gs] + ex_out_specs,
        out_shape=[jax.ShapeDtypeStruct((3, S, DGDN), F32), jax.ShapeDtypeStruct((S, LANES), F32)] + ex_out_shape,
        scratch_shapes=ex_scratch, compiler_params=_cparams(),
    )(qkv, qkv, qkv, gates, t_inv, du, dw, dqd, dkd, da, dgl, *ex_in)
    return res[0], res[1], res[2:]


def _scan_specs(nh, parts, reverse):
    wide, rows, chunks = nh * GHD, S // parts, NCH // parts

    def part(p):
        return parts - 1 - p if reverse else p

    hs = pl.BlockSpec((rows, wide), lambda g, p: (part(p), g))
    a_s = pl.BlockSpec((nh, rows, CHUNK), lambda g, p: (g, part(p), 0))
    gl_s = pl.BlockSpec((nh, chunks, 1, LANES), lambda g, p: (g, part(p), 0, 0))
    st_s = pl.BlockSpec((nh, chunks, GHD, GHD), lambda g, p: (g, part(p), 0, 0))
    gz_s = pl.BlockSpec((rows, wide), lambda g, p: (part(p), BLK_GZ // nh + g))
    mix_s = pl.BlockSpec((rows, wide), lambda g, p: (part(p), NPAIR // nh + g))
    return hs, a_s, gl_s, st_s, gz_s, mix_s


def _head_cols(hh):
    return slice(hh * GHD, (hh + 1) * GHD)


SCAN_HEADS, SCAN_PARTS = 4, 2
SCAN_HEADS_BWD, SCAN_PARTS_BWD = 4, 4


def _gdn_scan(u, w, qd, kd, a, gl, proj, w_norm, mix):
    heads = range(SCAN_HEADS)

    def body(u_ref, w_ref, qd_ref, kd_ref, a_ref, gl_ref, z_ref, wn_ref, mix_in, mix_ref, o_ref, st_ref, carry_ref):
        del mix_in

        @pl.when(pl.program_id(1) == 0)
        def _():
            carry_ref[...] = jnp.zeros_like(carry_ref)

        def step(ci, states):
            rows = pl.ds(pl.multiple_of(ci * CHUNK, CHUNK), CHUNK)
            for hh in heads:
                st_ref[hh, ci] = states[hh]
            ws = [_dot(w_ref[rows, _head_cols(hh)], states[hh]) for hh in heads]
            qs = [_dot(qd_ref[rows, _head_cols(hh)], states[hh]) for hh in heads]
            vn = [u_ref[rows, _head_cols(hh)] - ws[hh] for hh in heads]
            av = [_dot(a_ref[hh, rows, :], vn[hh]) for hh in heads]
            kv = [_dot(kd_ref[rows, _head_cols(hh)], vn[hh], 0, 0) for hh in heads]
            for hh in heads:
                o_ref[rows, _head_cols(hh)] = qs[hh] + av[hh]
            return tuple(states[hh] * gl_ref[hh, ci] + kv[hh] for hh in heads)

        last = lax.fori_loop(0, NCH // SCAN_PARTS, step, tuple(carry_ref[hh] for hh in heads))
        for hh in heads:
            carry_ref[hh] = last[hh]
            ov = o_ref[:, _head_cols(hh)]
            mix_ref[:, _head_cols(hh)] = (ov * _rms_scale(ov) * wn_ref[...] * _silu(z_ref[:, _head_cols(hh)])).astype(BF16)

    hs, a_s, gl_s, st_s, gz_s, mix_s = _scan_specs(SCAN_HEADS, SCAN_PARTS, False)
    return pl.pallas_call(
        body, name="gdn_scan", grid=(NGH // SCAN_HEADS, SCAN_PARTS),
        in_specs=[hs, hs, hs, hs, a_s, gl_s, gz_s, pl.BlockSpec((1, GHD), lambda g, p: (0, 0)),
                  pl.BlockSpec(memory_space=pl.ANY)],
        out_specs=[mix_s, hs, st_s],
        out_shape=[jax.ShapeDtypeStruct((S, D), BF16), jax.ShapeDtypeStruct((S, DGDN), F32),
                   jax.ShapeDtypeStruct((NGH, NCH, GHD, GHD), F32)],
        input_output_aliases={8: 0}, scratch_shapes=[pltpu.VMEM((SCAN_HEADS, GHD, GHD), F32)], compiler_params=_cparams(),
    )(u, w, qd, kd, a, gl, proj, w_norm, mix)


def _gdn_scan_bwd(dmix, o, proj, w_norm, u, w, qd, kd, a, gl, states, dproj, exchange=None):
    ex_in, ex_in_specs, ex_out_specs, ex_out_shape, ex_scratch = _hosted(exchange)
    groups = NGH // SCAN_HEADS_BWD

    def body(*refs):
        dy_ref, o_ref, z_ref, wn_ref, u_ref, w_ref, qd_ref, kd_ref, a_ref, gl_ref, st_ref = refs[:11]
        dz_ref, du_ref, dw_ref, dqd_ref, dkd_ref, da_ref, dgl_ref, dwn_ref = refs[12 + len(ex_in):20 + len(ex_in)]
        do_ref, carry_ref = refs[20 + len(ex_in) + len(ex_out_shape):22 + len(ex_in) + len(ex_out_shape)]
        ex_refs = refs[12:12 + len(ex_in)] + refs[20 + len(ex_in):20 + len(ex_in) + len(ex_out_shape)] + refs[-2:]
        heads = range(SCAN_HEADS_BWD)
        chunks = NCH // SCAN_PARTS_BWD

        if exchange is not None:
            @pl.when((pl.program_id(0) == 0) & (pl.program_id(1) == 0))
            def _():
                exchange.start(*exchange.split(ex_refs))

        @pl.when((pl.program_id(0) == 0) & (pl.program_id(1) == 0))
        def _():
            dwn_ref[...] = jnp.zeros_like(dwn_ref)

        @pl.when(pl.program_id(1) == 0)
        def _():
            carry_ref[...] = jnp.zeros_like(carry_ref)

        wn = wn_ref[...]
        for hh in heads:
            c = _head_cols(hh)
            ov = o_ref[:, c]
            zv = z_ref[:, c]
            g = dy_ref[:, c]
            sig = _sigmoid(zv)
            dz_ref[:, c] = (g * (ov * _rms_scale(ov) * wn) * sig * (1.0 + zv * (1.0 - sig))).astype(BF16)
            do, dwt = _rms_bwd(ov, wn, g * zv * sig)
            do_ref[:, c] = do
            dwn_ref[...] += jnp.sum(dwt, axis=0, keepdims=True)

        def step(t, dstates):
            ci = chunks - 1 - t
            rows = pl.ds(pl.multiple_of(ci * CHUNK, CHUNK), CHUNK)
            cols = [_head_cols(hh) for hh in heads]
            state = [st_ref[hh, ci] for hh in heads]
            dov = [do_ref[rows, cols[hh]] for hh in heads]
            wv = [w_ref[rows, cols[hh]] for hh in heads]
            ws = [_dot(wv[hh], state[hh]) for hh in heads]
            adov = [_dot(a_ref[hh, rows, :], dov[hh], 0, 0) for hh in heads]
            kds = [_dot(kd_ref[rows, cols[hh]], dstates[hh]) for hh in heads]
            dqd = [_dot(dov[hh], state[hh], 1, 1) for hh in heads]
            qdo = [_dot(qd_ref[rows, cols[hh]], dov[hh], 0, 0) for hh in heads]
            vn = [u_ref[rows, cols[hh]] - ws[hh] for hh in heads]
            dvn = [adov[hh] + kds[hh] for hh in heads]
            da = [_dot(dov[hh], vn[hh], 1, 1) for hh in heads]
            dkd = [_dot(vn[hh], dstates[hh], 1, 1) for hh in heads]
            dwv = [_dot(dvn[hh], state[hh], 1, 1) for hh in heads]
            wdv = [_dot(wv[hh], dvn[hh], 0, 0) for hh in heads]
            for hh in heads:
                da_ref[hh, rows, :] = da[hh]
                dqd_ref[rows, cols[hh]] = dqd[hh]
                dkd_ref[rows, cols[hh]] = dkd[hh]
                dgl = jnp.sum(jnp.sum(dstates[hh] * state[hh], axis=1, keepdims=True), axis=0, keepdims=True)
                dgl_ref[hh, ci] = jnp.broadcast_to(dgl, (1, LANES))
                du_ref[rows, cols[hh]] = dvn[hh]
                dw_ref[rows, cols[hh]] = -dwv[hh]
            return tuple(dstates[hh] * gl_ref[hh, ci] + qdo[hh] - wdv[hh] for hh in heads)

        last = lax.fori_loop(0, chunks, step, tuple(carry_ref[hh] for hh in heads))
        for hh in heads:
            carry_ref[hh] = last[hh]

        if exchange is not None:
            @pl.when((pl.program_id(0) == groups - 1) & (pl.program_id(1) == SCAN_PARTS_BWD - 1))
            def _():
                exchange.finish(*exchange.split(ex_refs))

    hs, a_s, gl_s, st_s, gz_s, mix_s = _scan_specs(SCAN_HEADS_BWD, SCAN_PARTS_BWD, True)
    vec = pl.BlockSpec((1, GHD), lambda g, p: (0, 0))
    tok = jax.ShapeDtypeStruct((S, DGDN), F32)
    res = pl.pallas_call(
        body, name="gdn_scan_bwd", grid=(groups, SCAN_PARTS_BWD),
        in_specs=[mix_s, hs, gz_s, vec, hs, hs, hs, hs, a_s, gl_s, st_s, pl.BlockSpec(memory_space=pl.ANY)] + ex_in_specs,
        out_specs=[gz_s, hs, hs, hs, hs, a_s, gl_s, vec] + ex_out_specs,
        out_shape=[jax.ShapeDtypeStruct((S, DPROJ_PAD), BF16), tok, tok, tok, tok,
                   jax.ShapeDtypeStruct((NGH, S, CHUNK), F32), jax.ShapeDtypeStruct((NGH, NCH, 1, LANES), F32),
                   jax.ShapeDtypeStruct((1, GHD), F32)] + ex_out_shape,
        input_output_aliases={11: 0},
        scratch_shapes=[pltpu.VMEM((S // SCAN_PARTS_BWD, SCAN_HEADS_BWD * GHD), F32),
                        pltpu.VMEM((SCAN_HEADS_BWD, GHD, GHD), F32)] + ex_scratch,
        compiler_params=_cparams(),
    )(dmix, o, proj, w_norm, u, w, qd, kd, a, gl, states, dproj, *ex_in)
    return res[:8], res[8:]


def _place():
    return lax.axis_index("x"), lax.axis_index("y"), lax.axis_index("c")


def _other_chips(x, y):
    return [(1 - x, y), (x, 1 - y), (1 - x, 1 - y)]


HBM = pl.BlockSpec(memory_space=pltpu.HBM)
VMEM = pl.BlockSpec(memory_space=pltpu.VMEM)


def _half_rows(ref_or_rows, half):
    rows = ref_or_rows // 2
    return pl.ds(pl.multiple_of(half * rows, rows), rows)


class _Exchange:
    def __init__(self, inputs, out_shape, n_sems, start, finish=None, middle=None, rest=None):
        self.inputs, self.out_shape, self.n_sems, self.start = inputs, out_shape, n_sems, start
        if finish is None:
            def finish(*refs):
                middle(*refs)
                rest(*refs)
        self.finish = finish
        self.middle = middle if middle is not None else (lambda *refs: None)
        self.rest = rest if rest is not None else finish

    def sem_shapes(self):
        return [pltpu.SemaphoreType.DMA((self.n_sems,)), pltpu.SemaphoreType.DMA((self.n_sems,))]

    def split(self, refs):
        n_in, n_out = len(self.inputs), len(self.out_shape)
        return refs[:n_in], refs[n_in:n_in + n_out], refs[n_in + n_out], refs[n_in + n_out + 1]


def _run_exchange(ex, name):
    def body(*refs):
        parts = ex.split(refs)
        ex.start(*parts)
        ex.finish(*parts)

    return pl.pallas_call(
        body, name=name, in_specs=[HBM] * len(ex.inputs), out_specs=[HBM] * len(ex.out_shape), out_shape=ex.out_shape,
        scratch_shapes=ex.sem_shapes(), compiler_params=_cparams(),
    )(*ex.inputs)


def _allgather_exchange(shards, whole=()):
    n, nw = len(shards), len(whole)
    slots = 8

    def plan(src, outs, send_sems, recv_sems):
        x, y, c = _place()
        via_x, via_y, diagonal = _other_chips(x, y)
        id_x, id_y, id_diagonal = [2 * chip[0] + chip[1] for chip in (via_x, via_y, diagonal)]
        me, sibling = (x, y, c), (x, y, 1 - c)

        def rows_of(a, half, quarter):
            total = src[a].shape[0]
            if quarter is None:
                return _half_rows(total, half)
            return pl.ds(pl.multiple_of(half * (total // 2) + quarter * (total // 4), total // 4), total // 4)

        def copy(a, k, chip_index, half, quarter, to, from_src=False):
            rows = rows_of(a, half, quarter)
            dst = outs[a].at[chip_index, rows]
            return pltpu.make_async_remote_copy(
                src_ref=src[a].at[rows] if from_src else dst, dst_ref=dst, send_sem=send_sems.at[slots * a + k],
                recv_sem=recv_sems.at[slots * a + k], device_id=to, device_id_type=MESH)

        def whole_copy(b, k, chip_index, to):
            return pltpu.make_async_remote_copy(
                src_ref=src[n + b], dst_ref=outs[n + b].at[chip_index], send_sem=send_sems.at[slots * n + 3 * b + k],
                recv_sem=recv_sems.at[slots * n + 3 * b + k], device_id=to, device_id_type=MESH)

        first, stages, last = [], [], []
        for a in range(n):
            first += [copy(a, 0, 2 * x + y, c, None, (*via_x, c), True), copy(a, 1, 2 * x + y, c, None, (*via_y, c), True)]
            stages.append([
                (copy(a, 0, id_x, c, None, me),
                 [copy(a, 2, id_x, c, 0, (*via_y, c)), copy(a, 4, id_x, c, None, sibling)]),
                (copy(a, 1, id_y, c, None, me),
                 [copy(a, 3, id_y, c, 1, (*via_x, c)), copy(a, 5, id_y, c, None, sibling)]),
                (copy(a, 2, id_diagonal, c, 0, me), [copy(a, 6, id_diagonal, c, 0, sibling)]),
                (copy(a, 3, id_diagonal, c, 1, me), [copy(a, 7, id_diagonal, c, 1, sibling)]),
            ])
            last += [copy(a, 4, id_x, 1 - c, None, me), copy(a, 5, id_y, 1 - c, None, me),
                     copy(a, 6, id_diagonal, 1 - c, 0, me), copy(a, 7, id_diagonal, 1 - c, 1, me)]
        for b in range(nw):
            for k, (chip, index) in enumerate(((via_x, id_x), (via_y, id_y), (diagonal, id_diagonal))):
                first.append(whole_copy(b, k, 2 * x + y, (*chip, c)))
                last.append(whole_copy(b, k, index, me))
        return first, stages, last

    def start(*refs):
        for cp in plan(*refs)[0]:
            cp.start()

    def pass_on(stages, which):
        for stage in which:
            for per_shard in stages:
                lands, onward = per_shard[stage]
                lands.wait_recv()
                for cp in onward:
                    cp.start()

    def middle(*refs):
        pass_on(plan(*refs)[1], (0, 1))

    def rest(*refs):
        first, stages, last = plan(*refs)
        pass_on(stages, (2, 3))
        for cp in last:
            cp.wait_recv()
        for cp in first + [cp for per_shard in stages for _, onward in per_shard for cp in onward]:
            cp.wait_send()

    out_shape = [jax.ShapeDtypeStruct((NCHIP,) + s.shape, s.dtype) for s in list(shards) + list(whole)]
    return _Exchange(list(shards) + list(whole), out_shape, slots * n + 3 * nw, start, middle=middle, rest=rest)


def _with_own(gathered, own):
    x, y, _ = _place()
    return lax.dynamic_update_index_in_dim(gathered, own, 2 * x + y, axis=0)


def _simple_exchange(inputs, out_shape, copies_of):
    def start(*refs):
        for cp in copies_of(*refs):
            cp.start()

    def finish(*refs):
        for cp in copies_of(*refs):
            cp.wait()

    return _Exchange(list(inputs), out_shape, len(out_shape) * 3, start, finish)


def _pair_exchange(grads):
    def copies_of(src, outs, send_sems, recv_sems):
        x, y, c = _place()
        return [pltpu.make_async_remote_copy(
            src_ref=src[a].at[:, _half_rows(src[a].shape[1], 1 - c)], dst_ref=outs[a], send_sem=send_sems.at[a],
            recv_sem=recv_sems.at[a], device_id=(x, y, 1 - c), device_id_type=MESH) for a in range(len(src))]

    return _simple_exchange(
        grads, [jax.ShapeDtypeStruct((g.shape[0], g.shape[1] // 2, g.shape[2]), g.dtype) for g in grads], copies_of)


def _pair_sum(grads, theirs, name):
    n = len(grads)

    def body(*refs):
        south = lax.axis_index("c") == 0
        for a in range(n):
            g = refs[a][...]
            half = g.shape[0] // 2
            mine = jnp.where(south, g[:half], g[half:])
            refs[2 * n + a][...] = (mine.astype(F32) + refs[n + a][...].astype(F32)).astype(BF16)

    def specs(arrs):
        return [pl.BlockSpec((None,) + g.shape[1:], lambda j: (j, 0, 0)) for g in arrs]

    return pl.pallas_call(
        body, name=name, grid=(NCHIP,), in_specs=specs(grads) + specs(theirs), out_specs=specs(theirs),
        out_shape=[jax.ShapeDtypeStruct(g.shape, BF16) for g in theirs], compiler_params=_cparams(),
    )(*grads, *theirs)


def _chip_exchange(parts):
    def copies_of(src, outs, send_sems, recv_sems):
        x, y, c = _place()
        return [pltpu.make_async_remote_copy(
            src_ref=src[a].at[2 * chip[0] + chip[1]], dst_ref=outs[a].at[k], send_sem=send_sems.at[3 * a + k],
            recv_sem=recv_sems.at[3 * a + k], device_id=(*chip, c), device_id_type=MESH)
            for a in range(len(src)) for k, chip in enumerate(_other_chips(x, y))]

    return _simple_exchange(parts, [jax.ShapeDtypeStruct((NCHIP - 1,) + p.shape[1:], p.dtype) for p in parts], copies_of)


def _chip_sum(parts, received, exchange=None):
    n = len(parts)
    steps = 4
    ex_in, ex_in_specs, ex_out_specs, ex_out_shape, ex_scratch = _hosted(exchange)

    def body(*refs):
        ex_refs = refs[2 * n:2 * n + len(ex_in)] + refs[3 * n + len(ex_in):]
        if exchange is not None:
            @pl.when(pl.program_id(0) == 0)
            def _():
                exchange.start(*exchange.split(ex_refs))

        chip = 2 * lax.axis_index("x") + lax.axis_index("y")
        for a in range(n):
            p, r = refs[a], refs[n + a]
            own = jnp.where(chip == 0, p[0], jnp.where(chip == 1, p[1], jnp.where(chip == 2, p[2], p[3])))
            refs[2 * n + len(ex_in) + a][...] = ((own.astype(F32) + r[0].astype(F32)) + r[1].astype(F32)) + r[2].astype(F32)

        if exchange is not None:
            @pl.when(pl.program_id(0) == steps - 1)
            def _():
                exchange.finish(*exchange.split(ex_refs))

    def specs(arrs):
        return [pl.BlockSpec((g.shape[0], g.shape[1] // steps, g.shape[2]), lambda i: (0, i, 0)) for g in arrs]

    out_specs = [pl.BlockSpec((g.shape[1] // steps, g.shape[2]), lambda i: (i, 0)) for g in parts]
    res = pl.pallas_call(
        body, name="grads_chip_sum", grid=(steps,), in_specs=specs(parts) + specs(received) + ex_in_specs,
        out_specs=out_specs + ex_out_specs, out_shape=[jax.ShapeDtypeStruct(g.shape[1:], F32) for g in parts] + ex_out_shape,
        scratch_shapes=ex_scratch, compiler_params=_cparams(),
    )(*parts, *received, *ex_in)
    return res[:n], res[n:]


def _pair_share(halves):
    def copies_of(src, outs, send_sems, recv_sems):
        x, y, c = _place()
        return [pltpu.make_async_remote_copy(
            src_ref=src[a], dst_ref=outs[a], send_sem=send_sems.at[a], recv_sem=recv_sems.at[a],
            device_id=(x, y, 1 - c), device_id_type=MESH) for a in range(len(src))]

    return _simple_exchange(halves, [jax.ShapeDtypeStruct(h.shape, F32) for h in halves], copies_of)


def _adamw_math(w, g, m, v):
    nm = ADAM_B1 * m + (1.0 - ADAM_B1) * g
    nv = ADAM_B2 * v + (1.0 - ADAM_B2) * jnp.square(g)
    m_hat = nm / (1.0 - ADAM_B1 ** ADAM_STEP)
    v_hat = nv / (1.0 - ADAM_B2 ** ADAM_STEP)
    return -ADAM_LR * (m_hat / (jnp.sqrt(v_hat) + ADAM_EPS) + ADAM_WD * w), nm, nv


def _adamw_big(ws, g_mine, g_theirs, ms, vs, exchange=None):
    n = len(ws)
    steps = 8
    ex_in, ex_in_specs, ex_out_specs, ex_out_shape, ex_scratch = _hosted(exchange)

    def body(*refs):
        ex_refs = refs[5 * n:5 * n + len(ex_in)] + refs[9 * n + len(ex_in):]
        outs = refs[5 * n + len(ex_in):9 * n + len(ex_in)]
        if exchange is not None:
            @pl.when(pl.program_id(0) == 0)
            def _():
                exchange.start(*exchange.split(ex_refs))

        own_half = (pl.program_id(0) // (steps // 2)) == lax.axis_index("c")
        for a in range(n):
            g = jnp.where(own_half, refs[n + a][...], refs[2 * n + a][...])
            d, nm, nv = _adamw_math(refs[a][...], g, refs[3 * n + a][...], refs[4 * n + a][...])
            outs[a][...] = g
            outs[n + a][...] = d
            outs[2 * n + a][...] = nm
            outs[3 * n + a][...] = nv

        if exchange is not None:
            @pl.when(pl.program_id(0) == steps - 1)
            def _():
                exchange.finish(*exchange.split(ex_refs))

    specs = [pl.BlockSpec((w.shape[0] // steps, w.shape[1]), lambda i: (i, 0)) for w in ws]
    half_specs = [pl.BlockSpec((g.shape[0] // (steps // 2), g.shape[1]), lambda i: (i % (steps // 2), 0)) for g in g_mine]
    shapes = [jax.ShapeDtypeStruct(w.shape, F32) for w in ws]
    res = pl.pallas_call(
        body, name="adamw_big", grid=(steps,), in_specs=specs + half_specs * 2 + specs * 2 + ex_in_specs,
        out_specs=specs * 4 + ex_out_specs, out_shape=shapes * 4 + ex_out_shape, scratch_shapes=ex_scratch,
        compiler_params=_cparams(),
    )(*ws, *g_mine, *g_theirs, *ms, *vs, *ex_in)
    return res[:n], res[n:2 * n], res[2 * n:3 * n], res[3 * n:4 * n], res[4 * n:]


def _adamw_in(w, g_mine, g_theirs, m, v):
    half = D // 2

    def body(w_ref, gm_ref, gt_ref, m_ref, v_ref, g_out, d_out, nm_out, nv_out, g_ref):
        south = lax.axis_index("c") == 0
        g_ref[0:half, :] = jnp.where(south, gm_ref[...], gt_ref[...])
        g_ref[half:D, :] = jnp.where(south, gt_ref[...], gm_ref[...])
        g = g_ref[0:CW, :]
        d, nm, nv = _adamw_math(w_ref[...], g, m_ref[...], v_ref[...])
        g_out[...] = g
        d_out[...] = d
        nm_out[...] = nm
        nv_out[...] = nv

    spec = pl.BlockSpec((CW, LANES), lambda i: (0, i))
    half_spec = pl.BlockSpec((half, LANES), lambda i: (0, i))
    return pl.pallas_call(
        body, name="adamw_in", grid=(D // LANES,), in_specs=[spec, half_spec, half_spec, spec, spec], out_specs=[spec] * 4,
        out_shape=[jax.ShapeDtypeStruct((CW, D), F32)] * 4, scratch_shapes=[pltpu.VMEM((D, LANES), F32)],
        compiler_params=_cparams(),
    )(w, g_mine, g_theirs, m, v)


NORM_NAMES = ("pre_mix_norm", "post_mix_norm", "pre_mlp_norm", "post_mlp_norm")
SMALL_NAMES = NORM_NAMES + ("gdn_conv_w", "fox_f_bias", "gdn_dt_bias", "gdn_a_log", "fox_out_norm", "gdn_out_norm")
CONV_COLS = 3 * DGDN // NCHIP


def _small_gather(d_norms, d_conv, sums, d_fox_norm, d_gdn_norm, loss_row):
    n_arrays = 6
    n_remote = n_arrays * (NDEV - 1)

    def copies_of(src, outs, send_sems, recv_sems):
        x, y, c = _place()
        me = 4 * x + 2 * y + c

        def from_me(chip_index):
            cols = pl.ds(pl.multiple_of(chip_index * CONV_COLS, LANES), CONV_COLS)
            return [src[0], src[1].at[:, cols], src[2], src[3], src[4], src[5]]

        local = [pltpu.make_async_copy(s, outs[a].at[me], send_sems.at[n_remote + a]) for a, s in enumerate(from_me(2 * x + y))]
        remote = []
        for k in range(1, NDEV):
            px, py, pc = x ^ ((k >> 2) & 1), y ^ ((k >> 1) & 1), c ^ (k & 1)
            remote += [pltpu.make_async_remote_copy(
                src_ref=s, dst_ref=outs[a].at[me], send_sem=send_sems.at[n_arrays * (k - 1) + a],
                recv_sem=recv_sems.at[n_arrays * (k - 1) + a], device_id=(px, py, pc), device_id_type=MESH)
                for a, s in enumerate(from_me(2 * px + py))]
        return local + remote

    def start(*refs):
        for cp in copies_of(*refs):
            cp.start()

    def finish(*refs):
        for cp in copies_of(*refs):
            cp.wait()

    shapes = [(4, D), (CONV_K, CONV_COLS), (8, LANES), (1, LANES), (1, LANES), (1, LANES)]
    return _Exchange([d_norms, d_conv, sums, d_fox_norm, d_gdn_norm, loss_row],
                     [jax.ShapeDtypeStruct((NDEV,) + s, F32) for s in shapes], n_remote + n_arrays, start, finish)


def _small_adamw(gathered, ws, ms, vs):
    n = len(SMALL_NAMES)
    ng = len(gathered)

    def body(*refs):
        def total(buf):
            acc = buf[0]
            for i in range(1, NDEV):
                acc = acc + buf[i]
            return acc

        t_norms, t_conv, t_sums, t_fn, t_gn, t_loss = [total(r) for r in refs[:ng]]
        w_refs, m_refs, v_refs = refs[ng:ng + n], refs[ng + n:ng + 2 * n], refs[ng + 2 * n:ng + 3 * n]
        outs = refs[ng + 3 * n:]
        outs[4 * n][...] = t_loss
        grads = [t_norms[i:i + 1, :] for i in range(4)] + [
            t_conv, t_sums[0:1, 0:NFH], t_sums[1:2, 0:NGH], t_sums[2:3, 0:NGH], t_fn[:, 0:FHD], t_gn]
        for a in range(n):
            d, nm, nv = _adamw_math(w_refs[a][...], grads[a], m_refs[a][...], v_refs[a][...])
            outs[a][...] = grads[a]
            outs[n + a][...] = d
            outs[2 * n + a][...] = nm
            outs[3 * n + a][...] = nv

    def whole(arr):
        return pl.BlockSpec(arr.shape, lambda i: (0,) * arr.ndim)

    res = pl.pallas_call(
        body, name="small_adamw", grid=(1,), in_specs=[whole(t) for t in gathered] + [whole(w) for w in ws] * 3,
        out_specs=[whole(w) for w in ws] * 4 + [pl.BlockSpec((1, LANES), lambda i: (0, 0))],
        out_shape=[jax.ShapeDtypeStruct(w.shape, F32) for w in ws] * 4 + [jax.ShapeDtypeStruct((1, LANES), F32)],
        compiler_params=_cparams(),
    )(*gathered, *ws, *ms, *vs)
    return res[:n], res[n:2 * n], res[2 * n:3 * n], res[3 * n:4 * n], res[4 * n]


CW = DPROJ // NCHIP
PROJ_RUNS = tuple((part * DFOX + hp * LANES, part * DFOX + (hp + 1) * LANES, (3 * hp + part) * LANES)
                  for hp in range(NPAIR) for part in range(3)) + (
    (1536, 1544, BLK_SMALL * LANES), (1544, 3080, BLK_GDN * LANES), (3080, 3088, BLK_SMALL * LANES + 8),
    (3088, 3600, BLK_GZ * LANES))


def _proj_pieces():
    pieces = []
    for lo, hi, at in PROJ_RUNS:
        while lo < hi:
            j = lo // CW
            end = min(hi, (j + 1) * CW)
            pieces.append((j, lo - j * CW, at, end - lo))
            at, lo = at + end - lo, end
    return pieces


RT = 256


def _to_padded_rows(gathered):
    def body(src_ref, out_ref, blocks_ref, rows_ref):
        blocks_ref[...] = src_ref[...].astype(F32)
        rows_ref[...] = jnp.zeros_like(rows_ref)
        for j, start, at, n in _proj_pieces():
            rows_ref[at:at + n, :] = blocks_ref[j, start:start + n, :]
        out_ref[...] = rows_ref[...].astype(out_ref.dtype)

    return pl.pallas_call(
        body, name="proj_rows_in", grid=(D // RT,), in_specs=[pl.BlockSpec((NCHIP, D, RT), lambda i: (0, 0, i))],
        out_specs=pl.BlockSpec((DPROJ_PAD, RT), lambda i: (0, i)), out_shape=jax.ShapeDtypeStruct((DPROJ_PAD, D), gathered.dtype),
        scratch_shapes=[pltpu.VMEM((NCHIP, D, RT), F32), pltpu.VMEM((DPROJ_PAD, RT), F32)], compiler_params=_cparams(),
    )(gathered)


def _from_padded_rows(w):
    def body(src_ref, out_ref, rows_ref, blocks_ref):
        rows_ref[...] = src_ref[...].astype(F32)
        blocks_ref[...] = jnp.zeros_like(blocks_ref)
        for j, start, at, n in _proj_pieces():
            blocks_ref[j, start:start + n, :] = rows_ref[at:at + n, :]
        out_ref[...] = blocks_ref[...].astype(out_ref.dtype)

    return pl.pallas_call(
        body, name="proj_rows_out", grid=(D // RT,), in_specs=[pl.BlockSpec((DPROJ_PAD, RT), lambda i: (0, i))],
        out_specs=pl.BlockSpec((NCHIP, D, RT), lambda i: (0, 0, i)), out_shape=jax.ShapeDtypeStruct((NCHIP, D, D), w.dtype),
        scratch_shapes=[pltpu.VMEM((DPROJ_PAD, RT), F32), pltpu.VMEM((NCHIP, D, RT), F32)], compiler_params=_cparams(),
    )(w)


def _local_step(x, target, first_weights, late_weights, reduce_late, reduce_in, pre_mix_norm, fox_f_bias, fox_out_norm,
                gdn_a_log, gdn_dt_bias, gdn_out_norm, post_mix_norm, pre_mlp_norm, post_mlp_norm):
    bias_vec = jnp.zeros((1, LANES), F32).at[0, 0:NFH].set(fox_f_bias).at[0, LANE_G:LANE_G + NGH].set(gdn_dt_bias)
    alog_vec = jnp.zeros((1, LANES), F32).at[0, LANE_G:LANE_G + NGH].set(gdn_a_log)
    w2 = jnp.concatenate([fox_out_norm, fox_out_norm], axis=1)

    h, first = _pre_norm(x, pre_mix_norm, exchange=first_weights[0])
    win_p, conv_w = first_weights[1](first)
    proj = _matmul(h, win_p, tb=True, tm=2048, tn=768, tk=1024, name="mm_proj", exchange=late_weights[0])
    proj, late_a = proj if late_weights[0] is not None else (proj, [])
    gates = _gates(proj, bias_vec, alog_vec)
    mix, fox_o, lse, late_b = _fox_fwd(proj, gates, w2, exchange=late_weights[1])
    qkv = _gdn_pre(proj, conv_w)
    (u, w, qd, kd, a_intra, gl, t_inv), _ = _gdn_prep(qkv, gates)
    wout, wup3 = late_weights[3](late_a, late_b)
    mix, gdn_raw, states = _gdn_scan(u, w, qd, kd, a_intra, gl, proj, gdn_out_norm, mix)
    def post_mix(acc, xv, w_post, w_pre_mlp):
        x1v = xv + acc * _rms_scale(acc) * w_post
        return acc, x1v, x1v * _rms_scale(x1v) * w_pre_mlp

    mixed, x1, h2 = _matmul(mix, wout, tm=512, tn=D, tk=1024, out_dtypes=(F32, F32, BF16), name="mm_out",
                            extra=(x, post_mix_norm, pre_mlp_norm), epilogue=post_mix)

    def relu2(acc):
        r = jnp.maximum(acc, 0.0)
        return r, r * r

    up_act = _matmul(h2, wup3, b3=True, tm=1024, tn=1024, tk=1024, out_dtypes=(BF16, BF16), epilogue=relu2,
                     name="mm_up", exchange=late_weights[2])
    (up_relu, act), late_c = up_act if late_weights[2] is not None else (up_act, [])
    wdown = late_weights[4](late_c)
    y = _matmul(act, wdown, tm=1024, tk=DFF, name="mm_down")
    dx2, dy, d_post_mlp, loss_row = _loss_head(x1, y, post_mlp_norm, target)

    dwdown = _matmul(act, dy, ta=True, tm=1024, tn=1024, tk=2048, out_dtypes=(BF16,), name="mm_dwdown")

    def relu2_bwd(acc, r):
        return (acc * 2.0 * r.astype(F32),)

    dup = _matmul(dy, wdown, tb=True, tm=1024, tn=1024, tk=1024, out_dtypes=(BF16,), extra=(up_relu,), epilogue=relu2_bwd,
                  name="mm_dact")
    dwup3 = _matmul(h2, dup, ta=True, tm=1024, tn=1024, tk=2048, out_dtypes=(BF16,), o3=True, name="mm_dwup")
    dh2 = _matmul(dup, wup3, tb=True, b3=True, tm=1024, tk=DFF, name="mm_dh2")
    dx1, dmixed, d_pre_mlp, d_post_mix = _mid_bwd(dh2, x1, pre_mlp_norm, dx2, mixed, post_mix_norm)
    dwout = _matmul(mix, dmixed, ta=True, tm=1024, tn=1024, tk=2048, out_dtypes=(BF16,), name="mm_dwout")
    dmix = _matmul(dmixed, wout, tb=True, tm=2048, tk=1024, name="mm_dmix")

    dfox, delta, d_fox_norm, from_sibling = _fox_norm_bwd(fox_o, dmix, w2, exchange=reduce_late[0](dwout, dwup3, dwdown))
    dproj, dcum_fox, reduced_a = _fox_bwd(proj, dfox, gates, lse, delta, exchange=reduce_late[1](from_sibling))
    (dproj, du, dw, dqd, dkd, da, dgl, d_gdn_norm), reduced_b = _gdn_scan_bwd(
        dmix, gdn_raw, proj, gdn_out_norm, u, w, qd, kd, a_intra, gl, states, dproj, exchange=reduce_late[2]())
    dqkv, dgates_gdn, reduced_c = _gdn_prep_bwd(qkv, gates, t_inv, du, dw, dqd, dkd, da, dgl, exchange=reduce_late[3]())
    reduced_late = (reduced_a, reduced_b, reduced_c)
    dproj, d_conv = _gdn_pre_bwd(proj, conv_w, dqkv, dproj)
    dproj, sums = _gates_bwd(proj, bias_vec, alog_vec, dgates_gdn, dcum_fox, dproj)

    dwin_p = _matmul(dproj, h, ta=True, tm=1280, tn=1024, tk=2048, out_dtypes=(BF16,), name="mm_dwin")
    exchange_in = reduce_in(dwin_p)
    dh = _matmul(dproj, win_p, tm=1024, tk=DPROJ_PAD, name="mm_dh", exchange=exchange_in)
    dh, reduced_in = dh if exchange_in is not None else (dh, [])
    grad_x, d_pre_mix = _pre_norm_bwd(dh, x, pre_mix_norm, dx1)

    d_norms = jnp.concatenate([d_pre_mix, d_post_mix, d_pre_mlp, d_post_mlp], axis=0)
    return grad_x, (d_norms, d_conv, sums, d_fox_norm, d_gdn_norm, loss_row), reduced_late, reduced_in


def kernel(x, pre_mix_norm, w_in, fox_f_bias, fox_out_norm, gdn_conv_w, gdn_a_log, gdn_dt_bias, gdn_out_norm, w_out, post_mix_norm, pre_mlp_norm, w_up, w_down, post_mlp_norm, loss_target, m_pre_mix_norm, m_w_in, m_fox_f_bias, m_fox_out_norm, m_gdn_conv_w, m_gdn_a_log, m_gdn_dt_bias, m_gdn_out_norm, m_w_out, m_post_mix_norm, m_pre_mlp_norm, m_w_up, m_w_down, m_post_mlp_norm, v_pre_mix_norm, v_w_in, v_fox_f_bias, v_fox_out_norm, v_gdn_conv_w, v_gdn_a_log, v_gdn_dt_bias, v_gdn_out_norm, v_w_out, v_post_mix_norm, v_pre_mlp_norm, v_w_up, v_w_down, v_post_mlp_norm):
    weights = dict(pre_mix_norm=pre_mix_norm, w_in=w_in, fox_f_bias=fox_f_bias, fox_out_norm=fox_out_norm, gdn_conv_w=gdn_conv_w,
                   gdn_a_log=gdn_a_log, gdn_dt_bias=gdn_dt_bias, gdn_out_norm=gdn_out_norm, w_out=w_out, post_mix_norm=post_mix_norm,
                   pre_mlp_norm=pre_mlp_norm, w_up=w_up, w_down=w_down, post_mlp_norm=post_mlp_norm)
    m_in = dict(pre_mix_norm=m_pre_mix_norm, w_in=m_w_in, fox_f_bias=m_fox_f_bias, fox_out_norm=m_fox_out_norm, gdn_conv_w=m_gdn_conv_w,
                gdn_a_log=m_gdn_a_log, gdn_dt_bias=m_gdn_dt_bias, gdn_out_norm=m_gdn_out_norm, w_out=m_w_out, post_mix_norm=m_post_mix_norm,
                pre_mlp_norm=m_pre_mlp_norm, w_up=m_w_up, w_down=m_w_down, post_mlp_norm=m_post_mlp_norm)
    v_in = dict(pre_mix_norm=v_pre_mix_norm, w_in=v_w_in, fox_f_bias=v_fox_f_bias, fox_out_norm=v_fox_out_norm, gdn_conv_w=v_gdn_conv_w,
                gdn_a_log=v_gdn_a_log, gdn_dt_bias=v_gdn_dt_bias, gdn_out_norm=v_gdn_out_norm, w_out=v_w_out, post_mix_norm=v_post_mix_norm,
                pre_mlp_norm=v_pre_mlp_norm, w_up=v_w_up, w_down=v_w_down, post_mlp_norm=v_post_mlp_norm)
    order_w = ("pre_mix_norm", "w_in", "fox_f_bias", "fox_out_norm", "gdn_conv_w", "gdn_a_log", "gdn_dt_bias", "gdn_out_norm", "w_out",
               "post_mix_norm", "pre_mlp_norm", "w_up", "w_down", "post_mlp_norm")
    big = ("w_in", "w_out", "w_up", "w_down")

    def row(v):
        return v if v.ndim == 2 else v.reshape(1, -1)

    win_shard = jnp.pad(w_in.T.astype(BF16), ((0, D - CW), (0, 0)))

    def resolve_first(gathered):
        win_g, conv_g = gathered
        return (_to_padded_rows(_with_own(win_g, win_shard)),
                _with_own(conv_g, gdn_conv_w).transpose(1, 0, 2).reshape(CONV_K, 3 * DGDN))

    late_shards = [weights[n].astype(BF16) for n in big[1:]]

    gathered_down = []

    def resolve_out_up(gathered_out, gathered_mlp):
        gathered_down.append(gathered_mlp[1])
        return _with_own(gathered_out[0], late_shards[0]).reshape(D, D), _with_own(gathered_mlp[0], late_shards[1])

    def resolve_down(_):
        return _with_own(gathered_down[0], late_shards[2]).reshape(DFF, D)

    pair_sums, late_blocks = {}, []

    def pair_summed(names, blocks, theirs):
        for n, s in zip(names, _pair_sum(blocks, theirs, "grads_pair_sum_" + names[0])):
            pair_sums[n] = s

    def late_pair_exchange(dwout, dwup3, dwdown):
        late_blocks.extend([dwout.reshape(NCHIP, D // NCHIP, D), dwup3, dwdown.reshape(NCHIP, DFF // NCHIP, D)])
        return _pair_exchange(late_blocks)

    def late_chip_exchange(theirs):
        pair_summed(big[1:], late_blocks, theirs)
        return _chip_exchange([pair_sums["w_up"], pair_sums["w_down"]])

    def reduce_in(dwin_p):
        blocks = [_from_padded_rows(dwin_p)]
        pair_summed(big[:1], blocks, _run_exchange(_pair_exchange(blocks), "grads_pair_exchange_w_in"))
        return _chip_exchange([pair_sums["w_in"]])

    grad_x, small, received_late, received_in = _local_step(
        x[0], loss_target[0], (_allgather_exchange([win_shard], whole=[gdn_conv_w]), resolve_first),
        (_allgather_exchange(late_shards[:1]), _allgather_exchange(late_shards[1:]), None, resolve_out_up, resolve_down),
        (late_pair_exchange, late_chip_exchange, lambda: None, lambda: _chip_exchange([pair_sums["w_out"]])),
        reduce_in, row(pre_mix_norm), fox_f_bias, row(fox_out_norm), gdn_a_log, gdn_dt_bias,
        row(gdn_out_norm), row(post_mix_norm), row(pre_mlp_norm), row(post_mlp_norm))
    received_mlp, _, received_out = received_late

    g_mine, small_gathered = _chip_sum(
        [pair_sums[n] for n in big], list(received_in[:1]) + list(received_out[:1]) + list(received_mlp[:2]),
        exchange=_small_gather(*small))
    g_theirs = _run_exchange(_pair_share(g_mine), "grads_pair_share")

    g_big, d_big, nm_big, nv_big, _ = _adamw_big(
        [weights[n] for n in big[1:]], g_mine[1:], g_theirs[1:], [m_in[n] for n in big[1:]], [v_in[n] for n in big[1:]])
    in_t = _adamw_in(w_in.T, g_mine[0], g_theirs[0], m_w_in.T, v_w_in.T)
    g_small, d_small, nm_small, nv_small, loss_total = _small_adamw(
        small_gathered, [row(weights[n]) for n in SMALL_NAMES], [row(m_in[n]) for n in SMALL_NAMES],
        [row(v_in[n]) for n in SMALL_NAMES])

    grads, delta, new_m, new_v = {}, {}, {}, {}
    grads["w_in"], delta["w_in"], new_m["w_in"], new_v["w_in"] = [t.T for t in in_t]
    for i, n in enumerate(big[1:]):
        grads[n], delta[n], new_m[n], new_v[n] = g_big[i], d_big[i], nm_big[i], nv_big[i]
    for i, n in enumerate(SMALL_NAMES):
        shape = weights[n].shape
        grads[n], delta[n], new_m[n], new_v[n] = (g_small[i].reshape(shape), d_small[i].reshape(shape),
                                                  nm_small[i].reshape(shape), nv_small[i].reshape(shape))
    return (loss_total[0, 0], grad_x[None], *[grads[n] for n in order_w], *[delta[n] for n in order_w], *[new_m[n] for n in order_w],
            *[new_v[n] for n in order_w])
```

```python
import jax
import jax.numpy as jnp
from jax import lax
from jax.experimental import pallas as pl
from jax.experimental.pallas import tpu as pltpu

F32 = jnp.float32
BF16 = jnp.bfloat16
MESH = pl.DeviceIdType.MESH

S = 2048
D = 1024
NFH, FHD = 8, 64
NPAIR = NFH // 2
NGH, GHD = 4, 128
DFOX = NFH * FHD
DGDN = NGH * GHD
CHUNK = 64
NCH = S // CHUNK
CONV_K = 4
DFF = 4 * D
EPS = 1e-6
DPROJ = 3600
LANES = 128
DPROJ_PAD = 3840
BLK_GDN = 12
BLK_GZ = 24
BLK_SMALL = 28
NCHIP = 4
NDEV = 8
VMEM_LIMIT = 56 * 1024 * 1024

ADAM_LR = 0.001
ADAM_B1 = 0.9
ADAM_B2 = 0.999
ADAM_EPS = 1e-08
ADAM_WD = 0.01
ADAM_STEP = 10


def _cparams(**kw):
    return pltpu.CompilerParams(vmem_limit_bytes=VMEM_LIMIT, **kw)


def _dn(ca, cb):
    return (((ca,), (cb,)), ((), ()))


def _dot(a, b, ca=1, cb=0):
    return lax.dot_general(a.astype(BF16), b.astype(BF16), _dn(ca, cb), preferred_element_type=F32)


def _hdot(a, b, ca=1, cb=0):
    return lax.dot_general(a.astype(F32), b.astype(F32), _dn(ca, cb), precision=lax.Precision.HIGHEST,
                           preferred_element_type=F32)


def _dot3(a, b, ca=1, cb=0):
    a_hi, b_hi = a.astype(BF16), b.astype(BF16)
    a_lo, b_lo = (a - a_hi.astype(F32)).astype(BF16), (b - b_hi.astype(F32)).astype(BF16)
    dn = _dn(ca, cb)
    return (lax.dot_general(a_hi, b_hi, dn, preferred_element_type=F32)
            + (lax.dot_general(a_hi, b_lo, dn, preferred_element_type=F32)
               + lax.dot_general(a_lo, b_hi, dn, preferred_element_type=F32)))


@jax.custom_vjp
def _mm_nn(a, b):
    return _dot(a, b, 1, 0)


def _mm_nn_fwd(a, b):
    return _dot(a, b, 1, 0), (a, b)


def _mm_nn_bwd(res, g):
    a, b = res
    return _dot(g, b, 1, 1), _dot(a, g, 0, 0)


_mm_nn.defvjp(_mm_nn_fwd, _mm_nn_bwd)


@jax.custom_vjp
def _mm_nt(a, b):
    return _dot(a, b, 1, 1)


def _mm_nt_fwd(a, b):
    return _dot(a, b, 1, 1), (a, b)


def _mm_nt_bwd(res, g):
    a, b = res
    return _dot(g, b, 1, 0), _dot(g, a, 0, 0)


_mm_nt.defvjp(_mm_nt_fwd, _mm_nt_bwd)


@jax.custom_vjp
def _saved_inverse(m, t_inv):
    del m
    return t_inv


def _saved_inverse_fwd(m, t_inv):
    del m
    return t_inv, t_inv


def _saved_inverse_bwd(t_inv, g):
    return -_dot3(_dot3(t_inv, g, 0, 0), t_inv, 1, 1), jnp.zeros_like(t_inv)


_saved_inverse.defvjp(_saved_inverse_fwd, _saved_inverse_bwd)


def _sigmoid(z):
    return 1.0 / (1.0 + jnp.exp(-z))


def _softplus(z):
    return jnp.maximum(z, 0.0) + jnp.log(1.0 + jnp.exp(-jnp.abs(z)))


def _silu(z):
    return z * _sigmoid(z)


def _rms_scale(x):
    return lax.rsqrt(jnp.mean(x * x, axis=-1, keepdims=True) + EPS)


def _rms_bwd(x, w, g):
    r = _rms_scale(x)
    gw = g * w
    dx = r * gw - x * (r * r * r) * jnp.mean(gw * x, axis=-1, keepdims=True)
    return dx, g * x * r


def _matmul(a, b, *, name, ta=False, tb=False, tm=512, tn=512, tk=512, out_dtypes=(F32,), b3=False, o3=False,
            extra=(), epilogue=None, exchange=None, n_sums=0):
    m, k = (a.shape[1], a.shape[0]) if ta else a.shape
    if b3:
        n = b.shape[1] if tb else b.shape[0] * b.shape[2]
        kb = b.shape[0] * b.shape[2] if tb else b.shape[1]
    else:
        n, kb = (b.shape[0], b.shape[1]) if tb else (b.shape[1], b.shape[0])
    assert kb == k, (name, kb, k)
    tm, tn, tk = min(tm, m), min(tn, n), min(tk, k)
    assert m % tm == 0 and n % tn == 0 and k % tk == 0, (name, m, n, k, tm, tn, tk)
    nk = k // tk
    whole_k_blocks = b3 and tb and not ta and nk == 1 and b.shape[0] > 1
    n_extra = len(extra)
    n_out = len(out_dtypes)
    grid = (m // tm, n // tn, nk)
    ex_in, ex_in_specs, ex_out_specs, ex_out_shape, ex_scratch = _hosted(exchange)

    def body(*refs):
        a_ref, b_ref = refs[0], refs[1]
        extra_refs = refs[2:2 + n_extra]
        first_out = 2 + n_extra + len(ex_in)
        out_refs = refs[first_out:first_out + n_out]
        ex_refs = refs[2 + n_extra:first_out] + refs[first_out + n_out:first_out + n_out + len(ex_out_shape)] + refs[-2:]
        step = [pl.program_id(d) for d in range(3)]

        if exchange is not None:
            @pl.when((step[0] == 0) & (step[1] == 0) & (step[2] == 0))
            def _():
                exchange.start(*exchange.split(ex_refs))

        def finish(acc):
            outs = (acc,) if epilogue is None else epilogue(acc, *[r[...] for r in extra_refs])
            for o_ref, val in zip(out_refs[:n_out - n_sums], outs):
                o_ref[...] = val.astype(o_ref.dtype)
            for o_ref, val in zip(out_refs[n_out - n_sums:], outs[n_out - n_sums:]):
                @pl.when(step[0] == 0)
                def _(o_ref=o_ref, val=val):
                    o_ref[...] = val

                @pl.when(step[0] > 0)
                def _(o_ref=o_ref, val=val):
                    o_ref[...] += val

        if whole_k_blocks:
            width = b.shape[2]
            part = _dot(a_ref[:, 0:width], b_ref[0], 1, 1)
            for blk in range(1, b.shape[0]):
                part = part + _dot(a_ref[:, blk * width:(blk + 1) * width], b_ref[blk], 1, 1)
        else:
            part = _dot(a_ref[...], b_ref[...], 0 if ta else 1, 1 if tb else 0)
        if nk == 1:
            finish(part)
        else:
            acc_ref = refs[first_out + n_out + len(ex_out_shape)]

            @pl.when(step[2] == 0)
            def _():
                acc_ref[...] = part

            @pl.when(step[2] > 0)
            def _():
                acc_ref[...] += part

            @pl.when(step[2] == nk - 1)
            def _():
                finish(acc_ref[...])

        if exchange is not None:
            flat = (step[0] * grid[1] + step[1]) * nk + step[2]
            total = grid[0] * grid[1] * nk

            @pl.when(flat == total // 2)
            def _():
                exchange.middle(*exchange.split(ex_refs))

            @pl.when(flat == total - 1)
            def _():
                exchange.rest(*exchange.split(ex_refs))

    a_spec = pl.BlockSpec((tk, tm), lambda i, j, kk: (kk, i)) if ta else pl.BlockSpec((tm, tk), lambda i, j, kk: (i, kk))
    if whole_k_blocks:
        b_spec = pl.BlockSpec((b.shape[0], tn, b.shape[2]), lambda i, j, kk: (0, j, 0))
    elif b3 and tb:
        assert b.shape[2] == tk
        b_spec = pl.BlockSpec((None, tn, tk), lambda i, j, kk: (kk, j, 0))
    elif b3:
        assert b.shape[2] == tn
        b_spec = pl.BlockSpec((None, tk, tn), lambda i, j, kk: (j, kk, 0))
    elif tb:
        b_spec = pl.BlockSpec((tn, tk), lambda i, j, kk: (j, kk))
    else:
        b_spec = pl.BlockSpec((tk, tn), lambda i, j, kk: (kk, j))
    tile = pl.BlockSpec((tm, tn), lambda i, j, kk: (i, j))
    out_specs = [tile] * n_out
    out_shape = [jax.ShapeDtypeStruct((m, n), dt) for dt in out_dtypes]
    if o3:
        out_specs[0] = pl.BlockSpec((None, tm, tn), lambda i, j, kk: (j, i, 0))
        out_shape[0] = jax.ShapeDtypeStruct((n // tn, m, tn), out_dtypes[0])
    assert n_sums == 0 or tn == n
    for r in range(n_out - n_sums, n_out):
        out_specs[r] = pl.BlockSpec((1, tn), lambda i, j, kk: (0, 0))
        out_shape[r] = jax.ShapeDtypeStruct((1, n), out_dtypes[r])
    res = pl.pallas_call(
        body, name=name, grid=grid,
        in_specs=[a_spec, b_spec] + [tile if e.shape[0] == m else pl.BlockSpec((1, tn), lambda i, j, kk: (0, j)) for e in extra]
        + ex_in_specs, out_specs=out_specs + ex_out_specs,
        out_shape=out_shape + ex_out_shape,
        scratch_shapes=([pltpu.VMEM((tm, tn), F32)] if nk > 1 else []) + ex_scratch,
        compiler_params=_cparams(),
    )(a, b, *extra, *ex_in)
    if exchange is not None:
        return (res[0] if n_out == 1 else res[:n_out]), res[n_out:]
    return res[0] if n_out == 1 else res


TR = 256


def _row_spec(cols):
    return pl.BlockSpec((TR, cols), lambda i: (i, 0))


def _vec_spec(cols):
    return pl.BlockSpec((1, cols), lambda i: (0, 0))


def _pre_norm(x, w, exchange=None):
    ex_in, ex_in_specs, ex_out_specs, ex_out_shape, ex_scratch = _hosted(exchange)

    def body(*refs):
        x_ref, w_ref, h_ref = refs[0], refs[1], refs[2 + len(ex_in)]
        ex_refs = refs[2:2 + len(ex_in)] + refs[3 + len(ex_in):]
        if exchange is not None:
            @pl.when(pl.program_id(0) == 0)
            def _():
                exchange.start(*exchange.split(ex_refs))

        xv = x_ref[...]
        h_ref[...] = (xv * _rms_scale(xv) * w_ref[...]).astype(BF16)

        if exchange is not None:
            @pl.when(pl.program_id(0) == S // TR - 1)
            def _():
                exchange.finish(*exchange.split(ex_refs))

    res = pl.pallas_call(
        body, name="pre_norm", grid=(S // TR,), in_specs=[_row_spec(D), _vec_spec(D)] + ex_in_specs,
        out_specs=[_row_spec(D)] + ex_out_specs, out_shape=[jax.ShapeDtypeStruct((S, D), BF16)] + ex_out_shape,
        scratch_shapes=ex_scratch, compiler_params=_cparams(),
    )(x, w, *ex_in)
    return res[0], res[1:]


def _pre_norm_bwd(dh, x, w, dx1):
    def body(dh_ref, x_ref, w_ref, dx1_ref, dx_ref, dw_ref):
        i = pl.program_id(0)
        dxa, dwt = _rms_bwd(x_ref[...], w_ref[...], dh_ref[...])
        dx_ref[...] = dx1_ref[...] + dxa

        @pl.when(i == 0)
        def _():
            dw_ref[...] = jnp.zeros_like(dw_ref)

        dw_ref[...] += jnp.sum(dwt, axis=0, keepdims=True)

    return pl.pallas_call(
        body, name="pre_norm_bwd", grid=(S // TR,),
        in_specs=[_row_spec(D), _row_spec(D), _vec_spec(D), _row_spec(D)], out_specs=[_row_spec(D), _vec_spec(D)],
        out_shape=[jax.ShapeDtypeStruct((S, D), F32), jax.ShapeDtypeStruct((1, D), F32)], compiler_params=_cparams(),
    )(dh, x, w, dx1)


BQ = 512
NQ = S // BQ
LANE_BETA, LANE_G = 8, 12


def _gate_lanes(shape):
    lane = lax.broadcasted_iota(jnp.int32, shape, 1)
    return lane < LANE_BETA, (lane >= LANE_BETA) & (lane < LANE_G), (lane >= LANE_G) & (lane < LANE_G + NGH)


def _gates(proj, bias_vec, alog_vec):
    def body(s_ref, b_ref, a_ref, o_ref, carry_ref):
        i = pl.program_id(0)

        @pl.when(i == 0)
        def _():
            carry_ref[...] = jnp.zeros_like(carry_ref)

        z = s_ref[...] + b_ref[...]
        tail = jnp.log(1.0 + jnp.exp(-jnp.abs(z)))
        sp = jnp.maximum(z, 0.0) + tail
        lf = jnp.minimum(z, 0.0) - tail
        r = lax.broadcasted_iota(jnp.int32, (BQ, BQ), 0)
        c = lax.broadcasted_iota(jnp.int32, (BQ, BQ), 1)
        tri = (c <= r).astype(F32)
        cum = _hdot(tri, lf) + carry_ref[...]
        carry_ref[...] = cum[BQ - 1:BQ, :]
        is_fox, is_beta, is_g = _gate_lanes(z.shape)
        o_ref[...] = jnp.where(is_fox, cum, jnp.where(is_beta, _sigmoid(z), jnp.where(is_g, -jnp.exp(a_ref[...]) * sp, 0.0)))

    return pl.pallas_call(
        body, name="gates", grid=(NQ,),
        in_specs=[pl.BlockSpec((BQ, LANES), lambda i: (i, BLK_SMALL)), _vec_spec(LANES), _vec_spec(LANES)],
        out_specs=pl.BlockSpec((BQ, LANES), lambda i: (i, 0)), out_shape=jax.ShapeDtypeStruct((S, LANES), F32),
        scratch_shapes=[pltpu.VMEM((1, LANES), F32)], compiler_params=_cparams(),
    )(proj, bias_vec, alog_vec)


def _gates_bwd(proj, bias_vec, alog_vec, dgates_gdn, dcum_fox, dproj):
    def body(s_ref, b_ref, a_ref, dg_ref, dc_ref, dproj_in, dproj_ref, red_ref, carry_ref):
        del dproj_in
        i = pl.program_id(0)

        @pl.when(i == 0)
        def _():
            carry_ref[...] = jnp.zeros_like(carry_ref)
            red_ref[...] = jnp.zeros_like(red_ref)

        z = s_ref[...] + b_ref[...]
        dg = dg_ref[...] + dc_ref[...]
        r = lax.broadcasted_iota(jnp.int32, (BQ, BQ), 0)
        c = lax.broadcasted_iota(jnp.int32, (BQ, BQ), 1)
        upper = (c >= r).astype(F32)
        dlf = _hdot(upper, dg) + carry_ref[...]
        carry_ref[...] = dlf[0:1, :]
        sig = _sigmoid(z)
        g_scale = -jnp.exp(a_ref[...])
        is_fox, is_beta, is_g = _gate_lanes(z.shape)
        ds = jnp.where(is_fox, dlf * (1.0 - sig), jnp.where(is_beta, dg * sig * (1.0 - sig), jnp.where(is_g, dg * g_scale * sig, 0.0)))
        dproj_ref[:, 0:LANES] = ds.astype(BF16)
        dproj_ref[:, LANES:2 * LANES] = jnp.zeros((BQ, LANES), BF16)
        dalog = jnp.where(is_g, dg * g_scale * _softplus(z), 0.0)
        sums = jnp.sum(ds, axis=0, keepdims=True)
        red_ref[0:1, :] += jnp.where(is_fox[0:1], sums, 0.0)
        red_ref[1:2, :] += pltpu.roll(jnp.where(is_g[0:1], sums, 0.0), LANES - LANE_G, 1)
        red_ref[2:3, :] += pltpu.roll(jnp.sum(dalog, axis=0, keepdims=True), LANES - LANE_G, 1)

    blk = pl.BlockSpec((BQ, LANES), lambda i: (NQ - 1 - i, 0))
    return pl.pallas_call(
        body, name="gates_bwd", grid=(NQ,),
        in_specs=[pl.BlockSpec((BQ, LANES), lambda i: (NQ - 1 - i, BLK_SMALL)), _vec_spec(LANES), _vec_spec(LANES), blk, blk,
                  pl.BlockSpec(memory_space=pl.ANY)],
        out_specs=[pl.BlockSpec((BQ, 2 * LANES), lambda i: (NQ - 1 - i, BLK_SMALL // 2)), pl.BlockSpec((8, LANES), lambda i: (0, 0))],
        out_shape=[jax.ShapeDtypeStruct((S, DPROJ_PAD), BF16), jax.ShapeDtypeStruct((8, LANES), F32)],
        input_output_aliases={5: 0},
        scratch_shapes=[pltpu.VMEM((1, LANES), F32)], compiler_params=_cparams(),
    )(proj, bias_vec, alog_vec, dgates_gdn, dcum_fox, dproj)


FOX_SCALE = FHD ** -0.5
FOX_PAIRS = 2
FOX_PAIRS_BWD = 2


def _head_mask(e):
    lane = lax.broadcasted_iota(jnp.int32, (1, LANES), 1)
    return (lane >= e * FHD) & (lane < (e + 1) * FHD)


def _lane_col(vals, index):
    lane = lax.broadcasted_iota(jnp.int32, vals.shape, 1)
    return jnp.sum(jnp.where(lane == index, vals, 0.0), axis=1, keepdims=True)


def _sublane_row(vals, index):
    row = lax.broadcasted_iota(jnp.int32, vals.shape, 0)
    return jnp.sum(jnp.where(row == index, vals, 0.0), axis=0, keepdims=True)


def _pair_cols(c0, c1):
    lane = lax.broadcasted_iota(jnp.int32, (c0.shape[0], 2), 1)
    return jnp.where(lane == 0, c0, c1)


def _split3(x):
    hi = x.astype(BF16).astype(F32)
    rest = x - hi
    mid = rest.astype(BF16).astype(F32)
    return hi, mid, (rest - mid).astype(BF16).astype(F32)


def _fox_operand(vals, e, cum, is_query):
    lane = lax.broadcasted_iota(jnp.int32, (1, LANES), 1)
    base = (1 - e) * FHD
    parts = _split3(cum)
    own = jnp.where(_head_mask(e), vals * FOX_SCALE if is_query else vals, 0.0)
    cum_at, ones_at = (base, base + 3) if is_query else (base + 3, base)
    sign = 1.0 if is_query else -1.0
    out = own + jnp.where((lane >= ones_at) & (lane < ones_at + 3), 1.0, 0.0)
    for i, part in enumerate(parts):
        out = out + jnp.where(lane == cum_at + i, sign * part, 0.0)
    return out.astype(BF16)


def _causal_block():
    return lax.broadcasted_iota(jnp.int32, (BQ, BQ), 1) <= lax.broadcasted_iota(jnp.int32, (BQ, BQ), 0)


def _head_rms(o, masks):
    o2 = o * o
    r = [lax.rsqrt(jnp.sum(jnp.where(mk, o2, 0.0), axis=1, keepdims=True) * (1.0 / FHD) + EPS) for mk in masks]
    return jnp.where(masks[0], r[0], r[1])


def _hosted(exchange):
    if exchange is None:
        return [], [], [], [], []
    return (exchange.inputs, [HBM] * len(exchange.inputs), [HBM] * len(exchange.out_shape), exchange.out_shape,
            exchange.sem_shapes())


def _fox_fwd(proj, gates, w2, exchange=None):
    ex_in, ex_in_specs, ex_out_specs, ex_out_shape, ex_scratch = _hosted(exchange)

    n_in = 3 * FOX_PAIRS + 2
    heads = [(pp, e) for pp in range(FOX_PAIRS) for e in range(2)]

    def body(*refs):
        qkv_refs, g_ref, w_ref = refs[:3 * FOX_PAIRS], refs[3 * FOX_PAIRS], refs[3 * FOX_PAIRS + 1]
        mix_ref, o_ref, lse_ref = refs[n_in + len(ex_in):n_in + 3 + len(ex_in)]
        ka_ref, vb_ref = refs[n_in + 3 + len(ex_in) + len(ex_out_shape):n_in + 5 + len(ex_in) + len(ex_out_shape)]
        ex_refs = refs[n_in:n_in + len(ex_in)] + refs[n_in + 3 + len(ex_in):n_in + 3 + len(ex_in) + len(ex_out_shape)] + refs[-2:]
        grp, qi = pl.program_id(0), pl.program_id(1)

        def head_index(pp, e):
            return 2 * (FOX_PAIRS * grp + pp) + e

        if exchange is not None:
            @pl.when((grp == 0) & (qi == 0))
            def _():
                exchange.start(*exchange.split(ex_refs))

        @pl.when(qi == 0)
        def _():
            gt = g_ref[...]
            for pp in range(FOX_PAIRS):
                kv = qkv_refs[3 * pp + 1][...]
                for e in range(2):
                    ka_ref[2 * pp + e] = _fox_operand(kv, e, _lane_col(gt, head_index(pp, e)), False)
                vb_ref[pp] = qkv_refs[3 * pp + 2][...].astype(BF16)

        masks = [_head_mask(0), _head_mask(1)]
        gt = g_ref[pl.ds(pl.multiple_of(qi * BQ, BQ), BQ), :]
        qs = [_fox_operand(qkv_refs[3 * pp][...], e, _lane_col(gt, head_index(pp, e)), True) for pp, e in heads]
        n = range(len(heads))

        def block(kj, carry, diagonal):
            rows = pl.ds(pl.multiple_of(kj * BQ, BQ), BQ)
            s = [_dot(qs[i], ka_ref[i, rows, :], 1, 1) for i in n]
            if diagonal:
                s = [jnp.where(_causal_block(), s[i], -jnp.inf) for i in n]
            m_new = [jnp.maximum(carry[i][0], jnp.max(s[i], axis=-1, keepdims=True)) for i in n]
            p = [jnp.exp(s[i] - m_new[i]) for i in n]
            alpha = [jnp.exp(carry[i][0] - m_new[i]) for i in n]
            l_new = [alpha[i] * carry[i][1] + jnp.sum(p[i], axis=-1, keepdims=True) for i in n]
            pv = [_dot(p[i], vb_ref[heads[i][0], rows, :]) for i in n]
            return tuple((m_new[i], l_new[i], alpha[i] * carry[i][2] + pv[i]) for i in n)

        one = (jnp.full((BQ, 1), -jnp.inf, F32), jnp.zeros((BQ, 1), F32), jnp.zeros((BQ, LANES), F32))
        below = lax.fori_loop(0, qi, lambda kj, carry: block(kj, carry, False), (one,) * len(heads))
        done = block(qi, below, True)
        for pp in range(FOX_PAIRS):
            (m0, l0, a0), (m1, l1, a1) = done[2 * pp], done[2 * pp + 1]
            o = jnp.where(masks[0], a0 / l0, a1 / l1)
            cols = slice(pp * LANES, (pp + 1) * LANES)
            o_ref[:, cols] = o
            mix_ref[:, cols] = (o * _head_rms(o, masks) * w_ref[...]).astype(BF16)
            lse_ref[pp] = _pair_cols(m0 + jnp.log(l0), m1 + jnp.log(l1))

        if exchange is not None:
            @pl.when((grp == NPAIR // FOX_PAIRS // 2) & (qi == 0))
            def _():
                exchange.middle(*exchange.split(ex_refs))

            @pl.when((grp == NPAIR // FOX_PAIRS - 1) & (qi == NQ - 1))
            def _():
                exchange.rest(*exchange.split(ex_refs))

    qkv_specs = []
    for pp in range(FOX_PAIRS):
        qkv_specs.append(pl.BlockSpec((BQ, LANES), lambda g, i, pp=pp: (i, 3 * (FOX_PAIRS * g + pp))))
        qkv_specs.append(pl.BlockSpec((S, LANES), lambda g, i, pp=pp: (0, 3 * (FOX_PAIRS * g + pp) + 1)))
        qkv_specs.append(pl.BlockSpec((S, LANES), lambda g, i, pp=pp: (0, 3 * (FOX_PAIRS * g + pp) + 2)))
    blk = pl.BlockSpec((BQ, FOX_PAIRS * LANES), lambda g, i: (i, g))
    res = pl.pallas_call(
        body, name="fox_fwd", grid=(NPAIR // FOX_PAIRS, NQ),
        in_specs=qkv_specs + [pl.BlockSpec((S, LANES), lambda g, i: (0, 0)), pl.BlockSpec((1, LANES), lambda g, i: (0, 0))]
        + ex_in_specs,
        out_specs=[blk, blk, pl.BlockSpec((FOX_PAIRS, BQ, 2), lambda g, i: (g, i, 0))] + ex_out_specs,
        out_shape=[jax.ShapeDtypeStruct((S, D), BF16), jax.ShapeDtypeStruct((S, DFOX), F32),
                   jax.ShapeDtypeStruct((NPAIR, S, 2), F32)] + ex_out_shape,
        scratch_shapes=[pltpu.VMEM((2 * FOX_PAIRS, S, LANES), BF16), pltpu.VMEM((FOX_PAIRS, S, LANES), BF16)] + ex_scratch,
        compiler_params=_cparams(),
    )(*([proj] * (3 * FOX_PAIRS)), gates, w2, *ex_in)
    return res[0], res[1], res[2], res[3:]


def _fox_norm_bwd(o, dmix, w2, exchange=None):
    ex_in, ex_in_specs, ex_out_specs, ex_out_shape, ex_scratch = _hosted(exchange)

    def body(*refs):
        o_ref, g_ref, w_ref = refs[:3]
        do_ref, dl_ref, dw_ref = refs[3 + len(ex_in):6 + len(ex_in)]
        ex_refs = refs[3:3 + len(ex_in)] + refs[6 + len(ex_in):]
        hp, qi = pl.program_id(0), pl.program_id(1)

        if exchange is not None:
            @pl.when((hp == 0) & (qi == 0))
            def _():
                exchange.start(*exchange.split(ex_refs))

        masks = [_head_mask(0), _head_mask(1)]
        ov = o_ref[...]
        g = g_ref[...]
        r = _head_rms(ov, masks)
        gw = g * w_ref[...]
        gwo = gw * ov
        mean = [jnp.sum(jnp.where(mk, gwo, 0.0), axis=1, keepdims=True) * (1.0 / FHD) for mk in masks]
        do = r * gw - ov * (r * r * r) * jnp.where(masks[0], mean[0], mean[1])
        do_ref[...] = do.astype(BF16)
        doo = do * ov
        dl_ref[...] = _pair_cols(*[jnp.sum(jnp.where(mk, doo, 0.0), axis=1, keepdims=True) for mk in masks])

        @pl.when((hp == 0) & (qi == 0))
        def _():
            dw_ref[...] = jnp.zeros_like(dw_ref)

        dw_ref[...] += jnp.sum(g * ov * r, axis=0, keepdims=True)

        @pl.when((hp == NPAIR - 1) & (qi == NQ - 1))
        def _():
            dw = dw_ref[...]
            dw_ref[...] = dw + pltpu.roll(dw, FHD, 1)
            if exchange is not None:
                exchange.finish(*exchange.split(ex_refs))

    blk = pl.BlockSpec((BQ, LANES), lambda hp, i: (i, hp))
    vec = pl.BlockSpec((1, LANES), lambda hp, i: (0, 0))
    res = pl.pallas_call(
        body, name="fox_norm_bwd", grid=(NPAIR, NQ), in_specs=[blk, blk, vec] + ex_in_specs,
        out_specs=[blk, pl.BlockSpec((None, BQ, 2), lambda hp, i: (hp, i, 0)), vec] + ex_out_specs,
        out_shape=[jax.ShapeDtypeStruct((S, DFOX), BF16), jax.ShapeDtypeStruct((NPAIR, S, 2), F32),
                   jax.ShapeDtypeStruct((1, LANES), F32)] + ex_out_shape,
        scratch_shapes=ex_scratch, compiler_params=_cparams(),
    )(o, dmix, w2, *ex_in)
    return res[0], res[1], res[2], res[3:]


def _fox_bwd(proj, do, gates, lse, delta, exchange=None):
    ex_in, ex_in_specs, ex_out_specs, ex_out_shape, ex_scratch = _hosted(exchange)

    pg = FOX_PAIRS_BWD
    n_in = 3 * pg + 4
    heads = [(pp, e) for pp in range(pg) for e in range(2)]

    def body(*refs):
        qkv_refs = refs[:3 * pg]
        do_ref, g_ref, lse_ref, dl_ref = refs[3 * pg:n_in]
        dproj_ref, dc_ref = refs[n_in + len(ex_in):n_in + 2 + len(ex_in)]
        qa_ref, dq_ref = refs[n_in + 2 + len(ex_in) + len(ex_out_shape):n_in + 4 + len(ex_in) + len(ex_out_shape)]
        ex_refs = refs[n_in:n_in + len(ex_in)] + refs[n_in + 2 + len(ex_in):n_in + 2 + len(ex_in) + len(ex_out_shape)] + refs[-2:]
        grp, kj = pl.program_id(0), pl.program_id(1)

        def head_index(pp, e):
            return 2 * (pg * grp + pp) + e

        if exchange is not None:
            @pl.when((grp == 0) & (kj == 0))
            def _():
                exchange.start(*exchange.split(ex_refs))

        @pl.when(kj == 0)
        def _():
            gt = g_ref[...]
            for pp in range(pg):
                qv = qkv_refs[3 * pp][...]
                for e in range(2):
                    qa_ref[2 * pp + e] = _fox_operand(qv, e, _lane_col(gt, head_index(pp, e)), True)
            dq_ref[...] = jnp.zeros_like(dq_ref)

        @pl.when((grp == 0) & (kj == 0))
        def _():
            dc_ref[...] = jnp.zeros_like(dc_ref)

        masks = [_head_mask(0), _head_mask(1)]
        krows = pl.ds(pl.multiple_of(kj * BQ, BQ), BQ)
        gk = g_ref[krows, :]
        kas = [_fox_operand(qkv_refs[3 * pp + 1][...], e, _lane_col(gk, head_index(pp, e)), False) for pp, e in heads]
        vbs = [qkv_refs[3 * pp + 2][...].astype(BF16) for pp in range(pg)]
        lane = lax.broadcasted_iota(jnp.int32, (BQ, LANES), 1)
        n = range(len(heads))

        def block(qi, carry, diagonal):
            dks, dvs, css = carry
            rows = pl.ds(pl.multiple_of(qi * BQ, BQ), BQ)
            qa = [qa_ref[i, rows, :] for i in n]
            s = [_dot(qa[i], kas[i], 1, 1) for i in n]
            if diagonal:
                s = [jnp.where(_causal_block(), s[i], -jnp.inf) for i in n]
            dov = [do_ref[rows, pp * LANES:(pp + 1) * LANES] for pp in range(pg)]
            doe = [jnp.where(masks[e], dov[pp], jnp.zeros_like(dov[pp])) for pp, e in heads]
            lse2 = [lse_ref[pp, rows, :] for pp in range(pg)]
            dl2 = [dl_ref[pp, rows, :] for pp in range(pg)]
            p = [jnp.exp(s[i] - _lane_col(lse2[heads[i][0]], heads[i][1])) for i in n]
            dp = [_dot(doe[i], vbs[heads[i][0]], 1, 1) for i in n]
            ds = [p[i] * (dp[i] - _lane_col(dl2[heads[i][0]], heads[i][1])) for i in n]
            dv_part = [_dot(p[i], doe[i], 0, 0) for i in n]
            dk_part = [_dot(ds[i], jnp.where(masks[heads[i][1]], qa[i], jnp.zeros_like(qa[i])), 0, 0) for i in n]
            dq_part = [jnp.where(masks[heads[i][1]], _dot(ds[i], kas[i]), 0.0) for i in n]
            css = tuple(css[i] + jnp.sum(ds[i], axis=0, keepdims=True) for i in n)
            dc = jnp.zeros((BQ, LANES), F32)
            for i in n:
                dc = dc + jnp.where(lane == head_index(*heads[i]), jnp.sum(ds[i], axis=1, keepdims=True), 0.0)
            for pp in range(pg):
                dq_ref[pp, rows, :] += (dq_part[2 * pp] + dq_part[2 * pp + 1]) * FOX_SCALE
            dc_ref[rows, :] += dc
            dks = tuple(dks[pp] + dk_part[2 * pp] + dk_part[2 * pp + 1] for pp in range(pg))
            dvs = tuple(dvs[pp] + dv_part[2 * pp] + dv_part[2 * pp + 1] for pp in range(pg))
            return dks, dvs, css

        zero = jnp.zeros((BQ, LANES), F32)
        first = block(kj, ((zero,) * pg, (zero,) * pg, (jnp.zeros((1, BQ), F32),) * len(heads)), True)
        dks, dvs, css = lax.fori_loop(kj + 1, NQ, lambda qi, carry: block(qi, carry, False), first)
        r = lax.broadcasted_iota(jnp.int32, (BQ, BQ), 0)
        c = lax.broadcasted_iota(jnp.int32, (BQ, BQ), 1)
        dcol = jnp.zeros((BQ, LANES), F32)
        for i in n:
            col = jnp.sum(jnp.where(r == c, css[i], 0.0), axis=1, keepdims=True)
            dcol = dcol + jnp.where(lane == head_index(*heads[i]), col, 0.0)
        dc_ref[krows, :] -= dcol
        for pp in range(pg):
            base = 3 * pp * LANES
            dproj_ref[krows, base + LANES:base + 2 * LANES] = dks[pp].astype(BF16)
            dproj_ref[krows, base + 2 * LANES:base + 3 * LANES] = dvs[pp].astype(BF16)

        @pl.when(kj == NQ - 1)
        def _():
            for pp in range(pg):
                dproj_ref[:, 3 * pp * LANES:(3 * pp + 1) * LANES] = dq_ref[pp].astype(BF16)

        if exchange is not None:
            @pl.when((grp == NPAIR // pg // 2) & (kj == 0))
            def _():
                exchange.middle(*exchange.split(ex_refs))

            @pl.when((grp == NPAIR // pg - 1) & (kj == NQ - 1))
            def _():
                exchange.rest(*exchange.split(ex_refs))

    qkv_specs = []
    for pp in range(pg):
        qkv_specs.append(pl.BlockSpec((S, LANES), lambda g, j, pp=pp: (0, 3 * (pg * g + pp))))
        qkv_specs.append(pl.BlockSpec((BQ, LANES), lambda g, j, pp=pp: (j, 3 * (pg * g + pp) + 1)))
        qkv_specs.append(pl.BlockSpec((BQ, LANES), lambda g, j, pp=pp: (j, 3 * (pg * g + pp) + 2)))
    pair = pl.BlockSpec((pg, S, 2), lambda g, j: (g, 0, 0))
    res = pl.pallas_call(
        body, name="fox_bwd", grid=(NPAIR // pg, NQ),
        in_specs=qkv_specs + [pl.BlockSpec((S, pg * LANES), lambda g, j: (0, g)), pl.BlockSpec((S, LANES), lambda g, j: (0, 0)),
                              pair, pair] + ex_in_specs,
        out_specs=[pl.BlockSpec((S, 3 * pg * LANES), lambda g, j: (0, g)), pl.BlockSpec((S, LANES), lambda g, j: (0, 0))]
        + ex_out_specs,
        out_shape=[jax.ShapeDtypeStruct((S, DPROJ_PAD), BF16), jax.ShapeDtypeStruct((S, LANES), F32)] + ex_out_shape,
        scratch_shapes=[pltpu.VMEM((2 * pg, S, LANES), BF16), pltpu.VMEM((pg, S, LANES), F32)] + ex_scratch,
        compiler_params=_cparams(),
    )(*([proj] * (3 * pg)), do, gates, lse, delta, *ex_in)
    return res[0], res[1], res[2:]


NQKV = 3 * NGH
GDN_QSCALE = GHD ** -0.5


def _shift_down(x, s):
    if s == 0:
        return x
    row = lax.broadcasted_iota(jnp.int32, x.shape, 0)
    return jnp.where(row >= s, pltpu.roll(x, s, 0), 0.0)


def _shift_up(x, s):
    if s == 0:
        return x
    n = x.shape[0]
    row = lax.broadcasted_iota(jnp.int32, x.shape, 0)
    return jnp.where(row < n - s, pltpu.roll(x, n - s, 0), 0.0)


def _conv_taps(xv):
    return [_shift_down(xv, CONV_K - 1 - j) for j in range(CONV_K)]


def _conv_pre(taps, wv):
    pre = taps[CONV_K - 1] * wv[CONV_K - 1:CONV_K, :]
    for j in range(CONV_K - 1):
        pre = pre + taps[j] * wv[j:j + 1, :]
    return pre


def _l2_factors(b):
    return b < 2 * NGH, jnp.where(b < NGH, GDN_QSCALE, 1.0)


def _gdn_pre(proj, conv_w):
    def body(x_ref, w_ref, o_ref):
        b = pl.program_id(0)
        c = _silu(_conv_pre(_conv_taps(x_ref[...]), w_ref[...]))
        normed, scale = _l2_factors(b)
        rs = lax.rsqrt(jnp.sum(c * c, axis=-1, keepdims=True) + EPS)
        o_ref[...] = c * jnp.where(normed, rs, 1.0) * scale

    return pl.pallas_call(
        body, name="gdn_pre", grid=(NQKV,),
        in_specs=[pl.BlockSpec((S, GHD), lambda b: (0, BLK_GDN + b)), pl.BlockSpec((CONV_K, GHD), lambda b: (0, b))],
        out_specs=pl.BlockSpec((S, GHD), lambda b: (0, b)),
        out_shape=jax.ShapeDtypeStruct((S, NQKV * GHD), F32), compiler_params=_cparams(),
    )(proj, conv_w)


def _gdn_pre_bwd(proj, conv_w, dqkv, dproj):
    def body(x_ref, w_ref, dy_ref, dproj_in, dx_ref, dw_ref):
        del dproj_in
        b = pl.program_id(0)
        taps = _conv_taps(x_ref[...])
        wv = w_ref[...]
        pre = _conv_pre(taps, wv)
        sig = _sigmoid(pre)
        c = pre * sig
        normed, scale = _l2_factors(b)
        g = dy_ref[...] * scale
        rs = lax.rsqrt(jnp.sum(c * c, axis=-1, keepdims=True) + EPS)
        dc_n = rs * g - c * (rs * rs * rs) * jnp.sum(g * c, axis=-1, keepdims=True)
        dc = jnp.where(normed, dc_n, g)
        dpre = dc * sig * (1.0 + pre * (1.0 - sig))
        dx = dpre * wv[CONV_K - 1:CONV_K, :]
        for j in range(CONV_K - 1):
            dx = dx + _shift_up(dpre, CONV_K - 1 - j) * wv[j:j + 1, :]
        dx_ref[...] = dx.astype(BF16)
        for j in range(CONV_K):
            dw_ref[j:j + 1, :] = jnp.sum(dpre * taps[j], axis=0, keepdims=True)

    return pl.pallas_call(
        body, name="gdn_pre_bwd", grid=(NQKV,),
        in_specs=[pl.BlockSpec((S, GHD), lambda b: (0, BLK_GDN + b)), pl.BlockSpec((CONV_K, GHD), lambda b: (0, b)),
                  pl.BlockSpec((None, S, GHD), lambda b: (b // NGH, 0, b % NGH)), pl.BlockSpec(memory_space=pl.ANY)],
        out_specs=[pl.BlockSpec((S, GHD), lambda b: (0, BLK_GDN + b)), pl.BlockSpec((CONV_K, GHD), lambda b: (0, b))],
        out_shape=[jax.ShapeDtypeStruct((S, DPROJ_PAD), BF16), jax.ShapeDtypeStruct((CONV_K, NQKV * GHD), F32)],
        input_output_aliases={3: 0}, compiler_params=_cparams(),
    )(proj, conv_w, dqkv, dproj)


CB = 16
NCB = NCH // CB


def _chunk_prep(qs, ks, vs, gcols, bcols, t_saved=None):
    n = range(len(qs))
    r = lax.broadcasted_iota(jnp.int32, (CHUNK, CHUNK), 0)
    c = lax.broadcasted_iota(jnp.int32, (CHUNK, CHUNK), 1)
    incl = c <= r
    eye = (r == c).astype(F32)
    grow = [jnp.sum(gcols[i] * eye, axis=0, keepdims=True) for i in n]
    gc_col = [jnp.sum(jnp.where(incl, grow[i], 0.0), axis=1, keepdims=True) for i in n]
    gc_row = [jnp.sum(jnp.where(r <= c, gcols[i], 0.0), axis=0, keepdims=True) for i in n]
    decay = [jnp.exp(jnp.where(incl, gc_col[i] - gc_row[i], -jnp.inf)) for i in n]
    kb = [ks[i] * bcols[i] for i in n]
    vb = [vs[i] * bcols[i] for i in n]
    kk = [_mm_nt(kb[i], ks[i]) for i in n]
    m = [jnp.where(c < r, kk[i] * decay[i], 0.0) for i in n]
    if t_saved is None:
        t_inv = [eye - m[i] for i in n]
        p = [_dot3(m[i], m[i]) for i in n]
        for step in range(5):
            t_inv = [t_inv[i] + _dot3(t_inv[i], p[i]) for i in n]
            if step < 4:
                p = [_dot3(p[i], p[i]) for i in n]
    else:
        t_inv = [_saved_inverse(m[i], t_saved[i]) for i in n]
    egc = [jnp.exp(gc_col[i]) for i in n]
    u = [_mm_nn(t_inv[i], vb[i]) for i in n]
    w = [_mm_nn(t_inv[i], kb[i] * egc[i]) for i in n]
    qk = [_mm_nt(qs[i], ks[i]) for i in n]
    gc_last = [gc_col[i][CHUNK - 1:CHUNK, :] for i in n]
    return [(u[i], w[i], qk[i] * decay[i], qs[i] * egc[i], ks[i] * jnp.exp(gc_last[i] - gc_col[i]), jnp.exp(gc_last[i]),
             t_inv[i]) for i in n]


def _prep_specs():
    rows = CB * CHUNK
    qs = pl.BlockSpec((rows, GHD), lambda i, h: (i, h))
    ks = pl.BlockSpec((rows, GHD), lambda i, h: (i, NGH + h))
    vs = pl.BlockSpec((rows, GHD), lambda i, h: (i, 2 * NGH + h))
    gs = pl.BlockSpec((rows, LANES), lambda i, h: (i, 0))
    a_s = pl.BlockSpec((None, rows, CHUNK), lambda i, h: (h, i, 0))
    gl_s = pl.BlockSpec((None, CB, 1, LANES), lambda i, h: (h, i, 0, 0))
    return qs, ks, vs, gs, a_s, gl_s


def _gdn_prep(qkv, gates, exchange=None):
    ex_in, ex_in_specs, ex_out_specs, ex_out_shape, ex_scratch = _hosted(exchange)

    def body(*refs):
        q_ref, k_ref, v_ref, g_ref = refs[:4]
        u_ref, w_ref, qd_ref, kd_ref, a_ref, gl_ref, t_ref = refs[4 + len(ex_in):11 + len(ex_in)]
        ex_refs = refs[4:4 + len(ex_in)] + refs[11 + len(ex_in):]
        h = pl.program_id(1)

        if exchange is not None:
            @pl.when((pl.program_id(0) == 0) & (h == 0))
            def _():
                exchange.start(*exchange.split(ex_refs))

        chunks = [pl.ds(cidx * CHUNK, CHUNK) for cidx in range(CB)]
        gts = [g_ref[rows, :] for rows in chunks]
        outs = _chunk_prep([q_ref[rows, :] for rows in chunks], [k_ref[rows, :] for rows in chunks],
                           [v_ref[rows, :] for rows in chunks], [_lane_col(gt, LANE_G + h) for gt in gts],
                           [_lane_col(gt, LANE_BETA + h) for gt in gts])
        for cidx, rows in enumerate(chunks):
            u, w, a, qd, kd, gl, t_inv = outs[cidx]
            u_ref[rows, :] = u
            w_ref[rows, :] = w
            qd_ref[rows, :] = qd
            kd_ref[rows, :] = kd
            a_ref[rows, :] = a
            t_ref[rows, :] = t_inv
            gl_ref[cidx] = jnp.broadcast_to(gl, (1, LANES))

        if exchange is not None:
            @pl.when((pl.program_id(0) == NCB // 2) & (h == 0))
            def _():
                exchange.middle(*exchange.split(ex_refs))

            @pl.when((pl.program_id(0) == NCB - 1) & (h == NGH - 1))
            def _():
                exchange.rest(*exchange.split(ex_refs))

    qs, ks, vs, gs, a_s, gl_s = _prep_specs()
    tok = jax.ShapeDtypeStruct((S, DGDN), F32)
    sq = jax.ShapeDtypeStruct((NGH, S, CHUNK), F32)
    res = pl.pallas_call(
        body, name="gdn_prep", grid=(NCB, NGH), in_specs=[qs, ks, vs, gs] + ex_in_specs,
        out_specs=[qs, qs, qs, qs, a_s, gl_s, a_s] + ex_out_specs,
        out_shape=[tok, tok, tok, tok, sq, jax.ShapeDtypeStruct((NGH, NCH, 1, LANES), F32), sq] + ex_out_shape,
        scratch_shapes=ex_scratch, compiler_params=_cparams(),
    )(qkv, qkv, qkv, gates, *ex_in)
    return res[:7], res[7:]


def _gdn_prep_bwd(qkv, gates, t_inv, du, dw, dqd, dkd, da, dgl, exchange=None):
    ex_in, ex_in_specs, ex_out_specs, ex_out_shape, ex_scratch = _hosted(exchange)

    def body(*refs):
        q_ref, k_ref, v_ref, g_ref, t_ref, du_ref, dw_ref, dqd_ref, dkd_ref, da_ref, dgl_ref = refs[:11]
        dqkv_ref, dg_ref = refs[11 + len(ex_in):13 + len(ex_in)]
        ex_refs = refs[11:11 + len(ex_in)] + refs[13 + len(ex_in):]
        h = pl.program_id(1)

        if exchange is not None:
            @pl.when((pl.program_id(0) == 0) & (h == 0))
            def _():
                exchange.start(*exchange.split(ex_refs))

        @pl.when(h == 0)
        def _():
            dg_ref[...] = jnp.zeros_like(dg_ref)

        lane = lax.broadcasted_iota(jnp.int32, (CHUNK, LANES), 1)
        chunks = [pl.ds(cidx * CHUNK, CHUNK) for cidx in range(CB)]
        gts = [g_ref[rows, :] for rows in chunks]
        t_saved = [t_ref[rows, :] for rows in chunks]
        _, vjp = jax.vjp(lambda *args: [o[:6] for o in _chunk_prep(*args, t_saved=t_saved)],
                         [q_ref[rows, :] for rows in chunks], [k_ref[rows, :] for rows in chunks],
                         [v_ref[rows, :] for rows in chunks], [_lane_col(gt, LANE_G + h) for gt in gts],
                         [_lane_col(gt, LANE_BETA + h) for gt in gts])
        dqs, dks, dvs, dgcs, dbcs = vjp([(du_ref[rows, :], dw_ref[rows, :], da_ref[rows, :], dqd_ref[rows, :],
                                          dkd_ref[rows, :], dgl_ref[cidx][:, 0:1]) for cidx, rows in enumerate(chunks)])
        for cidx, rows in enumerate(chunks):
            dq, dk, dv, dgc, dbc = dqs[cidx], dks[cidx], dvs[cidx], dgcs[cidx], dbcs[cidx]
            dqkv_ref[0, rows, :] = dq
            dqkv_ref[1, rows, :] = dk
            dqkv_ref[2, rows, :] = dv
            dg_ref[rows, :] += jnp.where(lane == LANE_G + h, dgc, 0.0) + jnp.where(lane == LANE_BETA + h, dbc, 0.0)

        if exchange is not None:
            @pl.when((pl.program_id(0) == NCB - 1) & (h == NGH - 1))
            def _():
                exchange.finish(*exchange.split(ex_refs))

    qs, ks, vs, gs, a_s, gl_s = _prep_specs()
    res = pl.pallas_call(
        body, name="gdn_prep_bwd", grid=(NCB, NGH), in_specs=[qs, ks, vs, gs, a_s, qs, qs, qs, qs, a_s, gl_s] + ex_in_specs,
        out_specs=[pl.BlockSpec((3, CB * CHUNK, GHD), lambda i, h: (0, i, h)), gs] + ex_out_specs,
        out_shape=[jax.ShapeDtypeStruct((3, S, DGDN), F32), jax.ShapeDtypeStruct((S, LANES), F32)] + ex_out_shape,
        scratch_shapes=ex_scratch, compiler_params=_cparams(),
    )(qkv, qkv, qkv, gates, t_inv, du, dw, dqd, dkd, da, dgl, *ex_in)
    return res[0], res[1], res[2:]


def _scan_specs(nh, parts, reverse):
    wide, rows, chunks = nh * GHD, S // parts, NCH // parts

    def part(p):
        return parts - 1 - p if reverse else p

    hs = pl.BlockSpec((rows, wide), lambda g, p: (part(p), g))
    a_s = pl.BlockSpec((nh, rows, CHUNK), lambda g, p: (g, part(p), 0))
    gl_s = pl.BlockSpec((nh, chunks, 1, LANES), lambda g, p: (g, part(p), 0, 0))
    st_s = pl.BlockSpec((nh, chunks, GHD, GHD), lambda g, p: (g, part(p), 0, 0))
    gz_s = pl.BlockSpec((rows, wide), lambda g, p: (part(p), BLK_GZ // nh + g))
    mix_s = pl.BlockSpec((rows, wide), lambda g, p: (part(p), NPAIR // nh + g))
    return hs, a_s, gl_s, st_s, gz_s, mix_s


def _head_cols(hh):
    return slice(hh * GHD, (hh + 1) * GHD)


SCAN_HEADS, SCAN_PARTS = 4, 2
SCAN_HEADS_BWD, SCAN_PARTS_BWD = 4, 4


def _gdn_scan(u, w, qd, kd, a, gl, proj, w_norm, mix):
    heads = range(SCAN_HEADS)

    def body(u_ref, w_ref, qd_ref, kd_ref, a_ref, gl_ref, z_ref, wn_ref, mix_in, mix_ref, o_ref, st_ref, carry_ref):
        del mix_in

        @pl.when(pl.program_id(1) == 0)
        def _():
            carry_ref[...] = jnp.zeros_like(carry_ref)

        def step(ci, states):
            rows = pl.ds(pl.multiple_of(ci * CHUNK, CHUNK), CHUNK)
            for hh in heads:
                st_ref[hh, ci] = states[hh]
            ws = [_dot(w_ref[rows, _head_cols(hh)], states[hh]) for hh in heads]
            qs = [_dot(qd_ref[rows, _head_cols(hh)], states[hh]) for hh in heads]
            vn = [u_ref[rows, _head_cols(hh)] - ws[hh] for hh in heads]
            av = [_dot(a_ref[hh, rows, :], vn[hh]) for hh in heads]
            kv = [_dot(kd_ref[rows, _head_cols(hh)], vn[hh], 0, 0) for hh in heads]
            for hh in heads:
                o_ref[rows, _head_cols(hh)] = qs[hh] + av[hh]
            return tuple(states[hh] * gl_ref[hh, ci] + kv[hh] for hh in heads)

        last = lax.fori_loop(0, NCH // SCAN_PARTS, step, tuple(carry_ref[hh] for hh in heads))
        for hh in heads:
            carry_ref[hh] = last[hh]
            ov = o_ref[:, _head_cols(hh)]
            mix_ref[:, _head_cols(hh)] = (ov * _rms_scale(ov) * wn_ref[...] * _silu(z_ref[:, _head_cols(hh)])).astype(BF16)

    hs, a_s, gl_s, st_s, gz_s, mix_s = _scan_specs(SCAN_HEADS, SCAN_PARTS, False)
    return pl.pallas_call(
        body, name="gdn_scan", grid=(NGH // SCAN_HEADS, SCAN_PARTS),
        in_specs=[hs, hs, hs, hs, a_s, gl_s, gz_s, pl.BlockSpec((1, GHD), lambda g, p: (0, 0)),
                  pl.BlockSpec(memory_space=pl.ANY)],
        out_specs=[mix_s, hs, st_s],
        out_shape=[jax.ShapeDtypeStruct((S, D), BF16), jax.ShapeDtypeStruct((S, DGDN), F32),
                   jax.ShapeDtypeStruct((NGH, NCH, GHD, GHD), F32)],
        input_output_aliases={8: 0}, scratch_shapes=[pltpu.VMEM((SCAN_HEADS, GHD, GHD), F32)], compiler_params=_cparams(),
    )(u, w, qd, kd, a, gl, proj, w_norm, mix)


def _gdn_scan_bwd(dmix, o, proj, w_norm, u, w, qd, kd, a, gl, states, dproj, exchange=None):
    ex_in, ex_in_specs, ex_out_specs, ex_out_shape, ex_scratch = _hosted(exchange)
    groups = NGH // SCAN_HEADS_BWD

    def body(*refs):
        dy_ref, o_ref, z_ref, wn_ref, u_ref, w_ref, qd_ref, kd_ref, a_ref, gl_ref, st_ref = refs[:11]
        dz_ref, du_ref, dw_ref, dqd_ref, dkd_ref, da_ref, dgl_ref, dwn_ref = refs[12 + len(ex_in):20 + len(ex_in)]
        do_ref, carry_ref = refs[20 + len(ex_in) + len(ex_out_shape):22 + len(ex_in) + len(ex_out_shape)]
        ex_refs = refs[12:12 + len(ex_in)] + refs[20 + len(ex_in):20 + len(ex_in) + len(ex_out_shape)] + refs[-2:]
        heads = range(SCAN_HEADS_BWD)
        chunks = NCH // SCAN_PARTS_BWD

        if exchange is not None:
            @pl.when((pl.program_id(0) == 0) & (pl.program_id(1) == 0))
            def _():
                exchange.start(*exchange.split(ex_refs))

        @pl.when((pl.program_id(0) == 0) & (pl.program_id(1) == 0))
        def _():
            dwn_ref[...] = jnp.zeros_like(dwn_ref)

        @pl.when(pl.program_id(1) == 0)
        def _():
            carry_ref[...] = jnp.zeros_like(carry_ref)

        wn = wn_ref[...]
        for hh in heads:
            c = _head_cols(hh)
            ov = o_ref[:, c]
            zv = z_ref[:, c]
            g = dy_ref[:, c]
            sig = _sigmoid(zv)
            dz_ref[:, c] = (g * (ov * _rms_scale(ov) * wn) * sig * (1.0 + zv * (1.0 - sig))).astype(BF16)
            do, dwt = _rms_bwd(ov, wn, g * zv * sig)
            do_ref[:, c] = do
            dwn_ref[...] += jnp.sum(dwt, axis=0, keepdims=True)

        def step(t, dstates):
            ci = chunks - 1 - t
            rows = pl.ds(pl.multiple_of(ci * CHUNK, CHUNK), CHUNK)
            cols = [_head_cols(hh) for hh in heads]
            state = [st_ref[hh, ci] for hh in heads]
            dov = [do_ref[rows, cols[hh]] for hh in heads]
            wv = [w_ref[rows, cols[hh]] for hh in heads]
            ws = [_dot(wv[hh], state[hh]) for hh in heads]
            adov = [_dot(a_ref[hh, rows, :], dov[hh], 0, 0) for hh in heads]
            kds = [_dot(kd_ref[rows, cols[hh]], dstates[hh]) for hh in heads]
            dqd = [_dot(dov[hh], state[hh], 1, 1) for hh in heads]
            qdo = [_dot(qd_ref[rows, cols[hh]], dov[hh], 0, 0) for hh in heads]
            vn = [u_ref[rows, cols[hh]] - ws[hh] for hh in heads]
            dvn = [adov[hh] + kds[hh] for hh in heads]
            da = [_dot(dov[hh], vn[hh], 1, 1) for hh in heads]
            dkd = [_dot(vn[hh], dstates[hh], 1, 1) for hh in heads]
            dwv = [_dot(dvn[hh], state[hh], 1, 1) for hh in heads]
            wdv = [_dot(wv[hh], dvn[hh], 0, 0) for hh in heads]
            for hh in heads:
                da_ref[hh, rows, :] = da[hh]
                dqd_ref[rows, cols[hh]] = dqd[hh]
                dkd_ref[rows, cols[hh]] = dkd[hh]
                dgl = jnp.sum(jnp.sum(dstates[hh] * state[hh], axis=1, keepdims=True), axis=0, keepdims=True)
                dgl_ref[hh, ci] = jnp.broadcast_to(dgl, (1, LANES))
                du_ref[rows, cols[hh]] = dvn[hh]
                dw_ref[rows, cols[hh]] = -dwv[hh]
            return tuple(dstates[hh] * gl_ref[hh, ci] + qdo[hh] - wdv[hh] for hh in heads)

        last = lax.fori_loop(0, chunks, step, tuple(carry_ref[hh] for hh in heads))
        for hh in heads:
            carry_ref[hh] = last[hh]

        if exchange is not None:
            @pl.when((pl.program_id(0) == groups - 1) & (pl.program_id(1) == SCAN_PARTS_BWD - 1))
            def _():
                exchange.finish(*exchange.split(ex_refs))

    hs, a_s, gl_s, st_s, gz_s, mix_s = _scan_specs(SCAN_HEADS_BWD, SCAN_PARTS_BWD, True)
    vec = pl.BlockSpec((1, GHD), lambda g, p: (0, 0))
    tok = jax.ShapeDtypeStruct((S, DGDN), F32)
    res = pl.pallas_call(
        body, name="gdn_scan_bwd", grid=(groups, SCAN_PARTS_BWD),
        in_specs=[mix_s, hs, gz_s, vec, hs, hs, hs, hs, a_s, gl_s, st_s, pl.BlockSpec(memory_space=pl.ANY)] + ex_in_specs,
        out_specs=[gz_s, hs, hs, hs, hs, a_s, gl_s, vec] + ex_out_specs,
        out_shape=[jax.ShapeDtypeStruct((S, DPROJ_PAD), BF16), tok, tok, tok, tok,
                   jax.ShapeDtypeStruct((NGH, S, CHUNK), F32), jax.ShapeDtypeStruct((NGH, NCH, 1, LANES), F32),
                   jax.ShapeDtypeStruct((1, GHD), F32)] + ex_out_shape,
        input_output_aliases={11: 0},
        scratch_shapes=[pltpu.VMEM((S // SCAN_PARTS_BWD, SCAN_HEADS_BWD * GHD), F32),
                        pltpu.VMEM((SCAN_HEADS_BWD, GHD, GHD), F32)] + ex_scratch,
        compiler_params=_cparams(),
    )(dmix, o, proj, w_norm, u, w, qd, kd, a, gl, states, dproj, *ex_in)
    return res[:8], res[8:]


def _place():
    return lax.axis_index("x"), lax.axis_index("y"), lax.axis_index("c")


def _other_chips(x, y):
    return [(1 - x, y), (x, 1 - y), (1 - x, 1 - y)]


HBM = pl.BlockSpec(memory_space=pltpu.HBM)
VMEM = pl.BlockSpec(memory_space=pltpu.VMEM)


def _half_rows(ref_or_rows, half):
    rows = ref_or_rows // 2
    return pl.ds(pl.multiple_of(half * rows, rows), rows)


class _Exchange:
    def __init__(self, inputs, out_shape, n_sems, start, finish=None, middle=None, rest=None):
        self.inputs, self.out_shape, self.n_sems, self.start = inputs, out_shape, n_sems, start
        if finish is None:
            def finish(*refs):
                middle(*refs)
                rest(*refs)
        self.finish = finish
        self.middle = middle if middle is not None else (lambda *refs: None)
        self.rest = rest if rest is not None else finish

    def sem_shapes(self):
        return [pltpu.SemaphoreType.DMA((self.n_sems,)), pltpu.SemaphoreType.DMA((self.n_sems,))]

    def split(self, refs):
        n_in, n_out = len(self.inputs), len(self.out_shape)
        return refs[:n_in], refs[n_in:n_in + n_out], refs[n_in + n_out], refs[n_in + n_out + 1]


def _run_exchange(ex, name):
    def body(*refs):
        parts = ex.split(refs)
        ex.start(*parts)
        ex.finish(*parts)

    return pl.pallas_call(
        body, name=name, in_specs=[HBM] * len(ex.inputs), out_specs=[HBM] * len(ex.out_shape), out_shape=ex.out_shape,
        scratch_shapes=ex.sem_shapes(), compiler_params=_cparams(),
    )(*ex.inputs)


def _allgather_exchange(shards, whole=()):
    n, nw = len(shards), len(whole)
    slots = 8

    def plan(src, outs, send_sems, recv_sems):
        x, y, c = _place()
        via_x, via_y, diagonal = _other_chips(x, y)
        id_x, id_y, id_diagonal = [2 * chip[0] + chip[1] for chip in (via_x, via_y, diagonal)]
        me, sibling = (x, y, c), (x, y, 1 - c)

        def rows_of(a, half, quarter):
            total = src[a].shape[0]
            if quarter is None:
                return _half_rows(total, half)
            return pl.ds(pl.multiple_of(half * (total // 2) + quarter * (total // 4), total // 4), total // 4)

        def copy(a, k, chip_index, half, quarter, to, from_src=False):
            rows = rows_of(a, half, quarter)
            dst = outs[a].at[chip_index, rows]
            return pltpu.make_async_remote_copy(
                src_ref=src[a].at[rows] if from_src else dst, dst_ref=dst, send_sem=send_sems.at[slots * a + k],
                recv_sem=recv_sems.at[slots * a + k], device_id=to, device_id_type=MESH)

        def whole_copy(b, k, chip_index, to):
            return pltpu.make_async_remote_copy(
                src_ref=src[n + b], dst_ref=outs[n + b].at[chip_index], send_sem=send_sems.at[slots * n + 3 * b + k],
                recv_sem=recv_sems.at[slots * n + 3 * b + k], device_id=to, device_id_type=MESH)

        first, stages, last = [], [], []
        for a in range(n):
            first += [copy(a, 0, 2 * x + y, c, None, (*via_x, c), True), copy(a, 1, 2 * x + y, c, None, (*via_y, c), True)]
            stages.append([
                (copy(a, 0, id_x, c, None, me),
                 [copy(a, 2, id_x, c, 0, (*via_y, c)), copy(a, 4, id_x, c, None, sibling)]),
                (copy(a, 1, id_y, c, None, me),
                 [copy(a, 3, id_y, c, 1, (*via_x, c)), copy(a, 5, id_y, c, None, sibling)]),
                (copy(a, 2, id_diagonal, c, 0, me), [copy(a, 6, id_diagonal, c, 0, sibling)]),
                (copy(a, 3, id_diagonal, c, 1, me), [copy(a, 7, id_diagonal, c, 1, sibling)]),
            ])
            last += [copy(a, 4, id_x, 1 - c, None, me), copy(a, 5, id_y, 1 - c, None, me),
                     copy(a, 6, id_diagonal, 1 - c, 0, me), copy(a, 7, id_diagonal, 1 - c, 1, me)]
        for b in range(nw):
            for k, (chip, index) in enumerate(((via_x, id_x), (via_y, id_y), (diagonal, id_diagonal))):
                first.append(whole_copy(b, k, 2 * x + y, (*chip, c)))
                last.append(whole_copy(b, k, index, me))
        return first, stages, last

    def start(*refs):
        for cp in plan(*refs)[0]:
            cp.start()

    def pass_on(stages, which):
        for stage in which:
            for per_shard in stages:
                lands, onward = per_shard[stage]
                lands.wait_recv()
                for cp in onward:
                    cp.start()

    def middle(*refs):
        pass_on(plan(*refs)[1], (0, 1))

    def rest(*refs):
        first, stages, last = plan(*refs)
        pass_on(stages, (2, 3))
        for cp in last:
            cp.wait_recv()
        for cp in first + [cp for per_shard in stages for _, onward in per_shard for cp in onward]:
            cp.wait_send()

    out_shape = [jax.ShapeDtypeStruct((NCHIP,) + s.shape, s.dtype) for s in list(shards) + list(whole)]
    return _Exchange(list(shards) + list(whole), out_shape, slots * n + 3 * nw, start, middle=middle, rest=rest)


def _with_own(gathered, own):
    x, y, _ = _place()
    return lax.dynamic_update_index_in_dim(gathered, own, 2 * x + y, axis=0)


def _simple_exchange(inputs, out_shape, copies_of):
    def start(*refs):
        for cp in copies_of(*refs):
            cp.start()

    def finish(*refs):
        for cp in copies_of(*refs):
            cp.wait()

    return _Exchange(list(inputs), out_shape, len(out_shape) * 3, start, finish)


def _pair_exchange(grads):
    def copies_of(src, outs, send_sems, recv_sems):
        x, y, c = _place()
        return [pltpu.make_async_remote_copy(
            src_ref=src[a].at[:, _half_rows(src[a].shape[1], 1 - c)], dst_ref=outs[a], send_sem=send_sems.at[a],
            recv_sem=recv_sems.at[a], device_id=(x, y, 1 - c), device_id_type=MESH) for a in range(len(src))]

    return _simple_exchange(
        grads, [jax.ShapeDtypeStruct((g.shape[0], g.shape[1] // 2, g.shape[2]), g.dtype) for g in grads], copies_of)


def _pair_sum(grads, theirs, name):
    n = len(grads)

    def body(*refs):
        south = lax.axis_index("c") == 0
        for a in range(n):
            g = refs[a][...]
            half = g.shape[0] // 2
            mine = jnp.where(south, g[:half], g[half:])
            refs[2 * n + a][...] = (mine.astype(F32) + refs[n + a][...].astype(F32)).astype(BF16)

    def specs(arrs):
        return [pl.BlockSpec((None,) + g.shape[1:], lambda j: (j, 0, 0)) for g in arrs]

    return pl.pallas_call(
        body, name=name, grid=(NCHIP,), in_specs=specs(grads) + specs(theirs), out_specs=specs(theirs),
        out_shape=[jax.ShapeDtypeStruct(g.shape, BF16) for g in theirs], compiler_params=_cparams(),
    )(*grads, *theirs)


def _chip_exchange(parts):
    def copies_of(src, outs, send_sems, recv_sems):
        x, y, c = _place()
        return [pltpu.make_async_remote_copy(
            src_ref=src[a].at[2 * chip[0] + chip[1]], dst_ref=outs[a].at[k], send_sem=send_sems.at[3 * a + k],
            recv_sem=recv_sems.at[3 * a + k], device_id=(*chip, c), device_id_type=MESH)
            for a in range(len(src)) for k, chip in enumerate(_other_chips(x, y))]

    return _simple_exchange(parts, [jax.ShapeDtypeStruct((NCHIP - 1,) + p.shape[1:], p.dtype) for p in parts], copies_of)


def _chip_sum(parts, received, exchange=None):
    n = len(parts)
    steps = 4
    ex_in, ex_in_specs, ex_out_specs, ex_out_shape, ex_scratch = _hosted(exchange)

    def body(*refs):
        ex_refs = refs[2 * n:2 * n + len(ex_in)] + refs[3 * n + len(ex_in):]
        if exchange is not None:
            @pl.when(pl.program_id(0) == 0)
            def _():
                exchange.start(*exchange.split(ex_refs))

        chip = 2 * lax.axis_index("x") + lax.axis_index("y")
        for a in range(n):
            p, r = refs[a], refs[n + a]
            own = jnp.where(chip == 0, p[0], jnp.where(chip == 1, p[1], jnp.where(chip == 2, p[2], p[3])))
            refs[2 * n + len(ex_in) + a][...] = ((own.astype(F32) + r[0].astype(F32)) + r[1].astype(F32)) + r[2].astype(F32)

        if exchange is not None:
            @pl.when(pl.program_id(0) == steps - 1)
            def _():
                exchange.finish(*exchange.split(ex_refs))

    def specs(arrs):
        return [pl.BlockSpec((g.shape[0], g.shape[1] // steps, g.shape[2]), lambda i: (0, i, 0)) for g in arrs]

    out_specs = [pl.BlockSpec((g.shape[1] // steps, g.shape[2]), lambda i: (i, 0)) for g in parts]
    res = pl.pallas_call(
        body, name="grads_chip_sum", grid=(steps,), in_specs=specs(parts) + specs(received) + ex_in_specs,
        out_specs=out_specs + ex_out_specs, out_shape=[jax.ShapeDtypeStruct(g.shape[1:], F32) for g in parts] + ex_out_shape,
        scratch_shapes=ex_scratch, compiler_params=_cparams(),
    )(*parts, *received, *ex_in)
    return res[:n], res[n:]


def _pair_share(halves):
    def copies_of(src, outs, send_sems, recv_sems):
        x, y, c = _place()
        return [pltpu.make_async_remote_copy(
            src_ref=src[a], dst_ref=outs[a], send_sem=send_sems.at[a], recv_sem=recv_sems.at[a],
            device_id=(x, y, 1 - c), device_id_type=MESH) for a in range(len(src))]

    return _simple_exchange(halves, [jax.ShapeDtypeStruct(h.shape, F32) for h in halves], copies_of)


def _adamw_math(w, g, m, v):
    nm = ADAM_B1 * m + (1.0 - ADAM_B1) * g
    nv = ADAM_B2 * v + (1.0 - ADAM_B2) * jnp.square(g)
    m_hat = nm / (1.0 - ADAM_B1 ** ADAM_STEP)
    v_hat = nv / (1.0 - ADAM_B2 ** ADAM_STEP)
    return -ADAM_LR * (m_hat / (jnp.sqrt(v_hat) + ADAM_EPS) + ADAM_WD * w), nm, nv


def _adamw_big(ws, g_mine, g_theirs, ms, vs, exchange=None):
    n = len(ws)
    steps = 8
    ex_in, ex_in_specs, ex_out_specs, ex_out_shape, ex_scratch = _hosted(exchange)

    def body(*refs):
        ex_refs = refs[5 * n:5 * n + len(ex_in)] + refs[9 * n + len(ex_in):]
        outs = refs[5 * n + len(ex_in):9 * n + len(ex_in)]
        if exchange is not None:
            @pl.when(pl.program_id(0) == 0)
            def _():
                exchange.start(*exchange.split(ex_refs))

        own_half = (pl.program_id(0) // (steps // 2)) == lax.axis_index("c")
        for a in range(n):
            g = jnp.where(own_half, refs[n + a][...], refs[2 * n + a][...])
            d, nm, nv = _adamw_math(refs[a][...], g, refs[3 * n + a][...], refs[4 * n + a][...])
            outs[a][...] = g
            outs[n + a][...] = d
            outs[2 * n + a][...] = nm
            outs[3 * n + a][...] = nv

        if exchange is not None:
            @pl.when(pl.program_id(0) == steps - 1)
            def _():
                exchange.finish(*exchange.split(ex_refs))

    specs = [pl.BlockSpec((w.shape[0] // steps, w.shape[1]), lambda i: (i, 0)) for w in ws]
    half_specs = [pl.BlockSpec((g.shape[0] // (steps // 2), g.shape[1]), lambda i: (i % (steps // 2), 0)) for g in g_mine]
    shapes = [jax.ShapeDtypeStruct(w.shape, F32) for w in ws]
    res = pl.pallas_call(
        body, name="adamw_big", grid=(steps,), in_specs=specs + half_specs * 2 + specs * 2 + ex_in_specs,
        out_specs=specs * 4 + ex_out_specs, out_shape=shapes * 4 + ex_out_shape, scratch_shapes=ex_scratch,
        compiler_params=_cparams(),
    )(*ws, *g_mine, *g_theirs, *ms, *vs, *ex_in)
    return res[:n], res[n:2 * n], res[2 * n:3 * n], res[3 * n:4 * n], res[4 * n:]


def _adamw_in(w, g_mine, g_theirs, m, v):
    half = D // 2

    def body(w_ref, gm_ref, gt_ref, m_ref, v_ref, g_out, d_out, nm_out, nv_out, g_ref):
        south = lax.axis_index("c") == 0
        g_ref[0:half, :] = jnp.where(south, gm_ref[...], gt_ref[...])
        g_ref[half:D, :] = jnp.where(south, gt_ref[...], gm_ref[...])
        g = g_ref[0:CW, :]
        d, nm, nv = _adamw_math(w_ref[...], g, m_ref[...], v_ref[...])
        g_out[...] = g
        d_out[...] = d
        nm_out[...] = nm
        nv_out[...] = nv

    spec = pl.BlockSpec((CW, LANES), lambda i: (0, i))
    half_spec = pl.BlockSpec((half, LANES), lambda i: (0, i))
    return pl.pallas_call(
        body, name="adamw_in", grid=(D // LANES,), in_specs=[spec, half_spec, half_spec, spec, spec], out_specs=[spec] * 4,
        out_shape=[jax.ShapeDtypeStruct((CW, D), F32)] * 4, scratch_shapes=[pltpu.VMEM((D, LANES), F32)],
        compiler_params=_cparams(),
    )(w, g_mine, g_theirs, m, v)


NORM_NAMES = ("pre_mix_norm", "post_mix_norm", "pre_mlp_norm", "post_mlp_norm")
SMALL_NAMES = NORM_NAMES + ("gdn_conv_w", "fox_f_bias", "gdn_dt_bias", "gdn_a_log", "fox_out_norm", "gdn_out_norm")
CONV_COLS = 3 * DGDN // NCHIP


def _small_gather(d_norms, d_conv, sums, d_fox_norm, d_gdn_norm, loss_row):
    n_arrays = 6
    n_remote = n_arrays * (NDEV - 1)

    def copies_of(src, outs, send_sems, recv_sems):
        x, y, c = _place()
        me = 4 * x + 2 * y + c

        def from_me(chip_index):
            cols = pl.ds(pl.multiple_of(chip_index * CONV_COLS, LANES), CONV_COLS)
            return [src[0], src[1].at[:, cols], src[2], src[3], src[4], src[5]]

        local = [pltpu.make_async_copy(s, outs[a].at[me], send_sems.at[n_remote + a]) for a, s in enumerate(from_me(2 * x + y))]
        remote = []
        for k in range(1, NDEV):
            px, py, pc = x ^ ((k >> 2) & 1), y ^ ((k >> 1) & 1), c ^ (k & 1)
            remote += [pltpu.make_async_remote_copy(
                src_ref=s, dst_ref=outs[a].at[me], send_sem=send_sems.at[n_arrays * (k - 1) + a],
                recv_sem=recv_sems.at[n_arrays * (k - 1) + a], device_id=(px, py, pc), device_id_type=MESH)
                for a, s in enumerate(from_me(2 * px + py))]
        return local + remote

    def start(*refs):
        for cp in copies_of(*refs):
            cp.start()

    def finish(*refs):
        for cp in copies_of(*refs):
            cp.wait()

    shapes = [(4, D), (CONV_K, CONV_COLS), (8, LANES), (1, LANES), (1, LANES), (1, LANES)]
    return _Exchange([d_norms, d_conv, sums, d_fox_norm, d_gdn_norm, loss_row],
                     [jax.ShapeDtypeStruct((NDEV,) + s, F32) for s in shapes], n_remote + n_arrays, start, finish)


def _small_adamw(gathered, ws, ms, vs):
    n = len(SMALL_NAMES)
    ng = len(gathered)

    def body(*refs):
        def total(buf):
            acc = buf[0]
            for i in range(1, NDEV):
                acc = acc + buf[i]
            return acc

        t_norms, t_conv, t_sums, t_fn, t_gn, t_loss = [total(r) for r in refs[:ng]]
        w_refs, m_refs, v_refs = refs[ng:ng + n], refs[ng + n:ng + 2 * n], refs[ng + 2 * n:ng + 3 * n]
        outs = refs[ng + 3 * n:]
        outs[4 * n][...] = t_loss
        grads = [t_norms[i:i + 1, :] for i in range(4)] + [
            t_conv, t_sums[0:1, 0:NFH], t_sums[1:2, 0:NGH], t_sums[2:3, 0:NGH], t_fn[:, 0:FHD], t_gn]
        for a in range(n):
            d, nm, nv = _adamw_math(w_refs[a][...], grads[a], m_refs[a][...], v_refs[a][...])
            outs[a][...] = grads[a]
            outs[n + a][...] = d
            outs[2 * n + a][...] = nm
            outs[3 * n + a][...] = nv

    def whole(arr):
        return pl.BlockSpec(arr.shape, lambda i: (0,) * arr.ndim)

    res = pl.pallas_call(
        body, name="small_adamw", grid=(1,), in_specs=[whole(t) for t in gathered] + [whole(w) for w in ws] * 3,
        out_specs=[whole(w) for w in ws] * 4 + [pl.BlockSpec((1, LANES), lambda i: (0, 0))],
        out_shape=[jax.ShapeDtypeStruct(w.shape, F32) for w in ws] * 4 + [jax.ShapeDtypeStruct((1, LANES), F32)],
        compiler_params=_cparams(),
    )(*gathered, *ws, *ms, *vs)
    return res[:n], res[n:2 * n], res[2 * n:3 * n], res[3 * n:4 * n], res[4 * n]


CW = DPROJ // NCHIP
PROJ_RUNS = tuple((part * DFOX + hp * LANES, part * DFOX + (hp + 1) * LANES, (3 * hp + part) * LANES)
                  for hp in range(NPAIR) for part in range(3)) + (
    (1536, 1544, BLK_SMALL * LANES), (1544, 3080, BLK_GDN * LANES), (3080, 3088, BLK_SMALL * LANES + 8),
    (3088, 3600, BLK_GZ * LANES))


def _proj_pieces():
    pieces = []
    for lo, hi, at in PROJ_RUNS:
        while lo < hi:
            j = lo // CW
            end = min(hi, (j + 1) * CW)
            pieces.append((j, lo - j * CW, at, end - lo))
            at, lo = at + end - lo, end
    return pieces


RT = 256


def _to_padded_rows(gathered):
    def body(src_ref, out_ref, blocks_ref, rows_ref):
        blocks_ref[...] = src_ref[...].astype(F32)
        rows_ref[...] = jnp.zeros_like(rows_ref)
        for j, start, at, n in _proj_pieces():
            rows_ref[at:at + n, :] = blocks_ref[j, start:start + n, :]
        out_ref[...] = rows_ref[...].astype(out_ref.dtype)

    return pl.pallas_call(
        body, name="proj_rows_in", grid=(D // RT,), in_specs=[pl.BlockSpec((NCHIP, D, RT), lambda i: (0, 0, i))],
        out_specs=pl.BlockSpec((DPROJ_PAD, RT), lambda i: (0, i)), out_shape=jax.ShapeDtypeStruct((DPROJ_PAD, D), gathered.dtype),
        scratch_shapes=[pltpu.VMEM((NCHIP, D, RT), F32), pltpu.VMEM((DPROJ_PAD, RT), F32)], compiler_params=_cparams(),
    )(gathered)


def _from_padded_rows(w):
    def body(src_ref, out_ref, rows_ref, blocks_ref):
        rows_ref[...] = src_ref[...].astype(F32)
        blocks_ref[...] = jnp.zeros_like(blocks_ref)
        for j, start, at, n in _proj_pieces():
            blocks_ref[j, start:start + n, :] = rows_ref[at:at + n, :]
        out_ref[...] = blocks_ref[...].astype(out_ref.dtype)

    return pl.pallas_call(
        body, name="proj_rows_out", grid=(D // RT,), in_specs=[pl.BlockSpec((DPROJ_PAD, RT), lambda i: (0, i))],
        out_specs=pl.BlockSpec((NCHIP, D, RT), lambda i: (0, 0, i)), out_shape=jax.ShapeDtypeStruct((NCHIP, D, D), w.dtype),
        scratch_shapes=[pltpu.VMEM((DPROJ_PAD, RT), F32), pltpu.VMEM((NCHIP, D, RT), F32)], compiler_params=_cparams(),
    )(w)


def _local_step(x, target, first_weights, late_weights, reduce_late, reduce_in, pre_mix_norm, fox_f_bias, fox_out_norm,
                gdn_a_log, gdn_dt_bias, gdn_out_norm, post_mix_norm, pre_mlp_norm, post_mlp_norm):
    bias_vec = jnp.zeros((1, LANES), F32).at[0, 0:NFH].set(fox_f_bias).at[0, LANE_G:LANE_G + NGH].set(gdn_dt_bias)
    alog_vec = jnp.zeros((1, LANES), F32).at[0, LANE_G:LANE_G + NGH].set(gdn_a_log)
    w2 = jnp.concatenate([fox_out_norm, fox_out_norm], axis=1)

    h, first = _pre_norm(x, pre_mix_norm, exchange=first_weights[0])
    win_p, conv_w = first_weights[1](first)
    proj = _matmul(h, win_p, tb=True, tm=2048, tn=768, tk=1024, name="mm_proj", exchange=late_weights[0])
    proj, late_a = proj if late_weights[0] is not None else (proj, [])
    gates = _gates(proj, bias_vec, alog_vec)
    mix, fox_o, lse, late_b = _fox_fwd(proj, gates, w2, exchange=late_weights[1])
    qkv = _gdn_pre(proj, conv_w)
    (u, w, qd, kd, a_intra, gl, t_inv), _ = _gdn_prep(qkv, gates)
    wout, wup3 = late_weights[3](late_a, late_b)
    mix, gdn_raw, states = _gdn_scan(u, w, qd, kd, a_intra, gl, proj, gdn_out_norm, mix)
    def post_mix(acc, xv, w_post, w_pre_mlp):
        x1v = xv + acc * _rms_scale(acc) * w_post
        return acc, x1v, x1v * _rms_scale(x1v) * w_pre_mlp

    mixed, x1, h2 = _matmul(mix, wout, tm=512, tn=D, tk=1024, out_dtypes=(F32, F32, BF16), name="mm_out",
                            extra=(x, post_mix_norm, pre_mlp_norm), epilogue=post_mix)

    def relu2(acc):
        r = jnp.maximum(acc, 0.0)
        return r, r * r

    up_act = _matmul(h2, wup3, b3=True, tm=1024, tn=1024, tk=1024, out_dtypes=(BF16, BF16), epilogue=relu2,
                     name="mm_up", exchange=late_weights[2])
    (up_relu, act), late_c = up_act if late_weights[2] is not None else (up_act, [])
    wdown = late_weights[4](late_c)
    def loss_head(acc, x1v, tv, w):
        err = x1v + acc * _rms_scale(acc) * w - tv
        dx2v = err * (1.0 / D)
        dyv, dwt = _rms_bwd(acc, w, dx2v)
        part = 0.5 * jnp.sum(jnp.mean(err * err, axis=-1, keepdims=True), axis=0, keepdims=True)
        return dx2v, dyv, jnp.sum(dwt, axis=0, keepdims=True), jnp.broadcast_to(part, (1, D))

    dx2, dy, d_post_mlp, loss_wide = _matmul(
        act, wdown, tm=512, tn=D, tk=DFF, out_dtypes=(F32, BF16, F32, F32), extra=(x1, target, post_mlp_norm),
        epilogue=loss_head, n_sums=2, name="mm_down")
    loss_row = loss_wide[:, :LANES]

    dwdown = _matmul(act, dy, ta=True, tm=1024, tn=1024, tk=2048, out_dtypes=(BF16,), name="mm_dwdown")

    def relu2_bwd(acc, r):
        return (acc * 2.0 * r.astype(F32),)

    dup = _matmul(dy, wdown, tb=True, tm=1024, tn=1024, tk=1024, out_dtypes=(BF16,), extra=(up_relu,), epilogue=relu2_bwd,
                  name="mm_dact")
    dwup3 = _matmul(h2, dup, ta=True, tm=1024, tn=1024, tk=2048, out_dtypes=(BF16,), o3=True, name="mm_dwup")
    def mid_bwd(acc, x1v, dx2v, mixedv, w_pre_mlp, w_post):
        dxa, dwm = _rms_bwd(x1v, w_pre_mlp, acc)
        dx1v = dx2v + dxa
        dm, dwp = _rms_bwd(mixedv, w_post, dx1v)
        return dx1v, dm, jnp.sum(dwm, axis=0, keepdims=True), jnp.sum(dwp, axis=0, keepdims=True)

    dx1, dmixed, d_pre_mlp, d_post_mix = _matmul(
        dup, wup3, tb=True, b3=True, tm=512, tn=D, tk=DFF, out_dtypes=(F32, BF16, F32, F32),
        extra=(x1, dx2, mixed, pre_mlp_norm, post_mix_norm), epilogue=mid_bwd, n_sums=2, name="mm_dh2")
    dwout = _matmul(mix, dmixed, ta=True, tm=1024, tn=1024, tk=2048, out_dtypes=(BF16,), name="mm_dwout")
    dmix = _matmul(dmixed, wout, tb=True, tm=2048, tk=1024, name="mm_dmix")

    dfox, delta, d_fox_norm, from_sibling = _fox_norm_bwd(fox_o, dmix, w2, exchange=reduce_late[0](dwout, dwup3, dwdown))
    dproj, dcum_fox, reduced_a = _fox_bwd(proj, dfox, gates, lse, delta, exchange=reduce_late[1](from_sibling))
    (dproj, du, dw, dqd, dkd, da, dgl, d_gdn_norm), reduced_b = _gdn_scan_bwd(
        dmix, gdn_raw, proj, gdn_out_norm, u, w, qd, kd, a_intra, gl, states, dproj, exchange=reduce_late[2]())
    dqkv, dgates_gdn, reduced_c = _gdn_prep_bwd(qkv, gates, t_inv, du, dw, dqd, dkd, da, dgl, exchange=reduce_late[3]())
    reduced_late = (reduced_a, reduced_b, reduced_c)
    dproj, d_conv = _gdn_pre_bwd(proj, conv_w, dqkv, dproj)
    dproj, sums = _gates_bwd(proj, bias_vec, alog_vec, dgates_gdn, dcum_fox, dproj)

    dwin_p = _matmul(dproj, h, ta=True, tm=1280, tn=1024, tk=2048, out_dtypes=(BF16,), name="mm_dwin")
    exchange_in = reduce_in(dwin_p)
    dh = _matmul(dproj, win_p, tm=1024, tk=DPROJ_PAD, name="mm_dh", exchange=exchange_in)
    dh, reduced_in = dh if exchange_in is not None else (dh, [])
    grad_x, d_pre_mix = _pre_norm_bwd(dh, x, pre_mix_norm, dx1)

    d_norms = jnp.concatenate([d_pre_mix, d_post_mix, d_pre_mlp, d_post_mlp], axis=0)
    return grad_x, (d_norms, d_conv, sums, d_fox_norm, d_gdn_norm, loss_row), reduced_late, reduced_in


def kernel(x, pre_mix_norm, w_in, fox_f_bias, fox_out_norm, gdn_conv_w, gdn_a_log, gdn_dt_bias, gdn_out_norm, w_out, post_mix_norm, pre_mlp_norm, w_up, w_down, post_mlp_norm, loss_target, m_pre_mix_norm, m_w_in, m_fox_f_bias, m_fox_out_norm, m_gdn_conv_w, m_gdn_a_log, m_gdn_dt_bias, m_gdn_out_norm, m_w_out, m_post_mix_norm, m_pre_mlp_norm, m_w_up, m_w_down, m_post_mlp_norm, v_pre_mix_norm, v_w_in, v_fox_f_bias, v_fox_out_norm, v_gdn_conv_w, v_gdn_a_log, v_gdn_dt_bias, v_gdn_out_norm, v_w_out, v_post_mix_norm, v_pre_mlp_norm, v_w_up, v_w_down, v_post_mlp_norm):
    weights = dict(pre_mix_norm=pre_mix_norm, w_in=w_in, fox_f_bias=fox_f_bias, fox_out_norm=fox_out_norm, gdn_conv_w=gdn_conv_w,
                   gdn_a_log=gdn_a_log, gdn_dt_bias=gdn_dt_bias, gdn_out_norm=gdn_out_norm, w_out=w_out, post_mix_norm=post_mix_norm,
                   pre_mlp_norm=pre_mlp_norm, w_up=w_up, w_down=w_down, post_mlp_norm=post_mlp_norm)
    m_in = dict(pre_mix_norm=m_pre_mix_norm, w_in=m_w_in, fox_f_bias=m_fox_f_bias, fox_out_norm=m_fox_out_norm, gdn_conv_w=m_gdn_conv_w,
                gdn_a_log=m_gdn_a_log, gdn_dt_bias=m_gdn_dt_bias, gdn_out_norm=m_gdn_out_norm, w_out=m_w_out, post_mix_norm=m_post_mix_norm,
                pre_mlp_norm=m_pre_mlp_norm, w_up=m_w_up, w_down=m_w_down, post_mlp_norm=m_post_mlp_norm)
    v_in = dict(pre_mix_norm=v_pre_mix_norm, w_in=v_w_in, fox_f_bias=v_fox_f_bias, fox_out_norm=v_fox_out_norm, gdn_conv_w=v_gdn_conv_w,
                gdn_a_log=v_gdn_a_log, gdn_dt_bias=v_gdn_dt_bias, gdn_out_norm=v_gdn_out_norm, w_out=v_w_out, post_mix_norm=v_post_mix_norm,
                pre_mlp_norm=v_pre_mlp_norm, w_up=v_w_up, w_down=v_w_down, post_mlp_norm=v_post_mlp_norm)
    order_w = ("pre_mix_norm", "w_in", "fox_f_bias", "fox_out_norm", "gdn_conv_w", "gdn_a_log", "gdn_dt_bias", "gdn_out_norm", "w_out",
               "post_mix_norm", "pre_mlp_norm", "w_up", "w_down", "post_mlp_norm")
    big = ("w_in", "w_out", "w_up", "w_down")

    def row(v):
        return v if v.ndim == 2 else v.reshape(1, -1)

    win_shard = jnp.pad(w_in.T.astype(BF16), ((0, D - CW), (0, 0)))

    def resolve_first(gathered):
        win_g, conv_g = gathered
        return (_to_padded_rows(_with_own(win_g, win_shard)),
                _with_own(conv_g, gdn_conv_w).transpose(1, 0, 2).reshape(CONV_K, 3 * DGDN))

    late_shards = [weights[n].astype(BF16) for n in big[1:]]

    gathered_down = []

    def resolve_out_up(gathered_out, gathered_mlp):
        gathered_down.append(gathered_mlp[1])
        return _with_own(gathered_out[0], late_shards[0]).reshape(D, D), _with_own(gathered_mlp[0], late_shards[1])

    def resolve_down(_):
        return _with_own(gathered_down[0], late_shards[2]).reshape(DFF, D)

    pair_sums, late_blocks = {}, []

    def pair_summed(names, blocks, theirs):
        for n, s in zip(names, _pair_sum(blocks, theirs, "grads_pair_sum_" + names[0])):
            pair_sums[n] = s

    def late_pair_exchange(dwout, dwup3, dwdown):
        late_blocks.extend([dwout.reshape(NCHIP, D // NCHIP, D), dwup3, dwdown.reshape(NCHIP, DFF // NCHIP, D)])
        return _pair_exchange(late_blocks)

    def late_chip_exchange(theirs):
        pair_summed(big[1:], late_blocks, theirs)
        return _chip_exchange([pair_sums["w_up"], pair_sums["w_down"]])

    def reduce_in(dwin_p):
        blocks = [_from_padded_rows(dwin_p)]
        pair_summed(big[:1], blocks, _run_exchange(_pair_exchange(blocks), "grads_pair_exchange_w_in"))
        return _chip_exchange([pair_sums["w_in"]])

    grad_x, small, received_late, received_in = _local_step(
        x[0], loss_target[0], (_allgather_exchange([win_shard], whole=[gdn_conv_w]), resolve_first),
        (_allgather_exchange(late_shards[:1]), _allgather_exchange(late_shards[1:]), None, resolve_out_up, resolve_down),
        (late_pair_exchange, late_chip_exchange, lambda: None, lambda: _chip_exchange([pair_sums["w_out"]])),
        reduce_in, row(pre_mix_norm), fox_f_bias, row(fox_out_norm), gdn_a_log, gdn_dt_bias,
        row(gdn_out_norm), row(post_mix_norm), row(pre_mlp_norm), row(post_mlp_norm))
    received_mlp, _, received_out = received_late

    g_mine, small_gathered = _chip_sum(
        [pair_sums[n] for n in big], list(received_in[:1]) + list(received_out[:1]) + list(received_mlp[:2]),
        exchange=_small_gather(*small))
    g_theirs = _run_exchange(_pair_share(g_mine), "grads_pair_share")

    g_big, d_big, nm_big, nv_big, _ = _adamw_big(
        [weights[n] for n in big[1:]], g_mine[1:], g_theirs[1:], [m_in[n] for n in big[1:]], [v_in[n] for n in big[1:]])
    in_t = _adamw_in(w_in.T, g_mine[0], g_theirs[0], m_w_in.T, v_w_in.T)
    g_small, d_small, nm_small, nv_small, loss_total = _small_adamw(
        small_gathered, [row(weights[n]) for n in SMALL_NAMES], [row(m_in[n]) for n in SMALL_NAMES],
        [row(v_in[n]) for n in SMALL_NAMES])

    grads, delta, new_m, new_v = {}, {}, {}, {}
    grads["w_in"], delta["w_in"], new_m["w_in"], new_v["w_in"] = [t.T for t in in_t]
    for i, n in enumerate(big[1:]):
        grads[n], delta[n], new_m[n], new_v[n] = g_big[i], d_big[i], nm_big[i], nv_big[i]
    for i, n in enumerate(SMALL_NAMES):
        shape = weights[n].shape
        grads[n], delta[n], new_m[n], new_v[n] = (g_small[i].reshape(shape), d_small[i].reshape(shape),
                                                  nm_small[i].reshape(shape), nv_small[i].reshape(shape))
    return (loss_total[0, 0], grad_x[None], *[grads[n] for n in order_w], *[delta[n] for n in order_w], *[new_m[n] for n in order_w],
            *[new_v[n] for n in order_w])
```

```python
import jax
import jax.numpy as jnp
from jax import lax
from jax.experimental import pallas as pl
from jax.experimental.pallas import tpu as pltpu

F32 = jnp.float32
BF16 = jnp.bfloat16
MESH = pl.DeviceIdType.MESH

S = 2048
D = 1024
NFH, FHD = 8, 64
NPAIR = NFH // 2
NGH, GHD = 4, 128
DFOX = NFH * FHD
DGDN = NGH * GHD
CHUNK = 64
NCH = S // CHUNK
CONV_K = 4
DFF = 4 * D
EPS = 1e-6
DPROJ = 3600
LANES = 128
DPROJ_PAD = 3840
BLK_GDN = 12
BLK_GZ = 24
BLK_SMALL = 28
NCHIP = 4
NDEV = 8
VMEM_LIMIT = 56 * 1024 * 1024

ADAM_LR = 0.001
ADAM_B1 = 0.9
ADAM_B2 = 0.999
ADAM_EPS = 1e-08
ADAM_WD = 0.01
ADAM_STEP = 10


def _cparams(**kw):
    return pltpu.CompilerParams(vmem_limit_bytes=VMEM_LIMIT, **kw)


def _dn(ca, cb):
    return (((ca,), (cb,)), ((), ()))


def _dot(a, b, ca=1, cb=0):
    return lax.dot_general(a.astype(BF16), b.astype(BF16), _dn(ca, cb), preferred_element_type=F32)


def _hdot(a, b, ca=1, cb=0):
    return lax.dot_general(a.astype(F32), b.astype(F32), _dn(ca, cb), precision=lax.Precision.HIGHEST,
                           preferred_element_type=F32)


def _dot3(a, b, ca=1, cb=0):
    a_hi, b_hi = a.astype(BF16), b.astype(BF16)
    a_lo, b_lo = (a - a_hi.astype(F32)).astype(BF16), (b - b_hi.astype(F32)).astype(BF16)
    dn = _dn(ca, cb)
    return (lax.dot_general(a_hi, b_hi, dn, preferred_element_type=F32)
            + (lax.dot_general(a_hi, b_lo, dn, preferred_element_type=F32)
               + lax.dot_general(a_lo, b_hi, dn, preferred_element_type=F32)))


@jax.custom_vjp
def _mm_nn(a, b):
    return _dot(a, b, 1, 0)


def _mm_nn_fwd(a, b):
    return _dot(a, b, 1, 0), (a, b)


def _mm_nn_bwd(res, g):
    a, b = res
    return _dot(g, b, 1, 1), _dot(a, g, 0, 0)


_mm_nn.defvjp(_mm_nn_fwd, _mm_nn_bwd)


@jax.custom_vjp
def _mm_nt(a, b):
    return _dot(a, b, 1, 1)


def _mm_nt_fwd(a, b):
    return _dot(a, b, 1, 1), (a, b)


def _mm_nt_bwd(res, g):
    a, b = res
    return _dot(g, b, 1, 0), _dot(g, a, 0, 0)


_mm_nt.defvjp(_mm_nt_fwd, _mm_nt_bwd)


@jax.custom_vjp
def _saved_inverse(m, t_inv):
    del m
    return t_inv


def _saved_inverse_fwd(m, t_inv):
    del m
    return t_inv, t_inv


def _saved_inverse_bwd(t_inv, g):
    return -_dot3(_dot3(t_inv, g, 0, 0), t_inv, 1, 1), jnp.zeros_like(t_inv)


_saved_inverse.defvjp(_saved_inverse_fwd, _saved_inverse_bwd)


def _sigmoid(z):
    return 1.0 / (1.0 + jnp.exp(-z))


def _softplus(z):
    return jnp.maximum(z, 0.0) + jnp.log(1.0 + jnp.exp(-jnp.abs(z)))


def _silu(z):
    return z * _sigmoid(z)


def _rms_scale(x):
    return lax.rsqrt(jnp.mean(x * x, axis=-1, keepdims=True) + EPS)


def _rms_bwd(x, w, g):
    r = _rms_scale(x)
    gw = g * w
    dx = r * gw - x * (r * r * r) * jnp.mean(gw * x, axis=-1, keepdims=True)
    return dx, g * x * r


def _matmul(a, b, *, name, ta=False, tb=False, tm=512, tn=512, tk=512, out_dtypes=(F32,), b3=False, o3=False,
            extra=(), epilogue=None, exchange=None, n_sums=0):
    m, k = (a.shape[1], a.shape[0]) if ta else a.shape
    if b3:
        n = b.shape[1] if tb else b.shape[0] * b.shape[2]
        kb = b.shape[0] * b.shape[2] if tb else b.shape[1]
    else:
        n, kb = (b.shape[0], b.shape[1]) if tb else (b.shape[1], b.shape[0])
    assert kb == k, (name, kb, k)
    tm, tn, tk = min(tm, m), min(tn, n), min(tk, k)
    assert m % tm == 0 and n % tn == 0 and k % tk == 0, (name, m, n, k, tm, tn, tk)
    nk = k // tk
    whole_k_blocks = b3 and tb and not ta and nk == 1 and b.shape[0] > 1
    n_extra = len(extra)
    n_out = len(out_dtypes)
    grid = (m // tm, n // tn, nk)
    ex_in, ex_in_specs, ex_out_specs, ex_out_shape, ex_scratch = _hosted(exchange)

    def body(*refs):
        a_ref, b_ref = refs[0], refs[1]
        extra_refs = refs[2:2 + n_extra]
        first_out = 2 + n_extra + len(ex_in)
        out_refs = refs[first_out:first_out + n_out]
        ex_refs = refs[2 + n_extra:first_out] + refs[first_out + n_out:first_out + n_out + len(ex_out_shape)] + refs[-2:]
        step = [pl.program_id(d) for d in range(3)]

        if exchange is not None:
            @pl.when((step[0] == 0) & (step[1] == 0) & (step[2] == 0))
            def _():
                exchange.start(*exchange.split(ex_refs))

        def finish(acc):
            outs = (acc,) if epilogue is None else epilogue(acc, *[r[...] for r in extra_refs])
            for o_ref, val in zip(out_refs[:n_out - n_sums], outs):
                o_ref[...] = val.astype(o_ref.dtype)
            for o_ref, val in zip(out_refs[n_out - n_sums:], outs[n_out - n_sums:]):
                @pl.when(step[0] == 0)
                def _(o_ref=o_ref, val=val):
                    o_ref[...] = val

                @pl.when(step[0] > 0)
                def _(o_ref=o_ref, val=val):
                    o_ref[...] += val

        if whole_k_blocks:
            width = b.shape[2]
            part = _dot(a_ref[:, 0:width], b_ref[0], 1, 1)
            for blk in range(1, b.shape[0]):
                part = part + _dot(a_ref[:, blk * width:(blk + 1) * width], b_ref[blk], 1, 1)
        else:
            part = _dot(a_ref[...], b_ref[...], 0 if ta else 1, 1 if tb else 0)
        if nk == 1:
            finish(part)
        else:
            acc_ref = refs[first_out + n_out + len(ex_out_shape)]

            @pl.when(step[2] == 0)
            def _():
                acc_ref[...] = part

            @pl.when(step[2] > 0)
            def _():
                acc_ref[...] += part

            @pl.when(step[2] == nk - 1)
            def _():
                finish(acc_ref[...])

        if exchange is not None:
            flat = (step[0] * grid[1] + step[1]) * nk + step[2]
            total = grid[0] * grid[1] * nk

            @pl.when(flat == total // 2)
            def _():
                exchange.middle(*exchange.split(ex_refs))

            @pl.when(flat == total - 1)
            def _():
                exchange.rest(*exchange.split(ex_refs))

    a_spec = pl.BlockSpec((tk, tm), lambda i, j, kk: (kk, i)) if ta else pl.BlockSpec((tm, tk), lambda i, j, kk: (i, kk))
    if whole_k_blocks:
        b_spec = pl.BlockSpec((b.shape[0], tn, b.shape[2]), lambda i, j, kk: (0, j, 0))
    elif b3 and tb:
        assert b.shape[2] == tk
        b_spec = pl.BlockSpec((None, tn, tk), lambda i, j, kk: (kk, j, 0))
    elif b3:
        assert b.shape[2] == tn
        b_spec = pl.BlockSpec((None, tk, tn), lambda i, j, kk: (j, kk, 0))
    elif tb:
        b_spec = pl.BlockSpec((tn, tk), lambda i, j, kk: (j, kk))
    else:
        b_spec = pl.BlockSpec((tk, tn), lambda i, j, kk: (kk, j))
    tile = pl.BlockSpec((tm, tn), lambda i, j, kk: (i, j))
    out_specs = [tile] * n_out
    out_shape = [jax.ShapeDtypeStruct((m, n), dt) for dt in out_dtypes]
    if o3:
        out_specs[0] = pl.BlockSpec((None, tm, tn), lambda i, j, kk: (j, i, 0))
        out_shape[0] = jax.ShapeDtypeStruct((n // tn, m, tn), out_dtypes[0])
    assert n_sums == 0 or tn == n
    for r in range(n_out - n_sums, n_out):
        out_specs[r] = pl.BlockSpec((1, tn), lambda i, j, kk: (0, 0))
        out_shape[r] = jax.ShapeDtypeStruct((1, n), out_dtypes[r])
    res = pl.pallas_call(
        body, name=name, grid=grid,
        in_specs=[a_spec, b_spec] + [tile if e.shape[0] == m else pl.BlockSpec((1, tn), lambda i, j, kk: (0, j)) for e in extra]
        + ex_in_specs, out_specs=out_specs + ex_out_specs,
        out_shape=out_shape + ex_out_shape,
        scratch_shapes=([pltpu.VMEM((tm, tn), F32)] if nk > 1 else []) + ex_scratch,
        compiler_params=_cparams(),
    )(a, b, *extra, *ex_in)
    if exchange is not None:
        return (res[0] if n_out == 1 else res[:n_out]), res[n_out:]
    return res[0] if n_out == 1 else res


TR = 256


def _row_spec(cols):
    return pl.BlockSpec((TR, cols), lambda i: (i, 0))


def _vec_spec(cols):
    return pl.BlockSpec((1, cols), lambda i: (0, 0))


def _pre_norm(x, w, exchange=None):
    ex_in, ex_in_specs, ex_out_specs, ex_out_shape, ex_scratch = _hosted(exchange)

    def body(*refs):
        x_ref, w_ref, h_ref = refs[0], refs[1], refs[2 + len(ex_in)]
        ex_refs = refs[2:2 + len(ex_in)] + refs[3 + len(ex_in):]
        if exchange is not None:
            @pl.when(pl.program_id(0) == 0)
            def _():
                exchange.start(*exchange.split(ex_refs))

        xv = x_ref[...]
        h_ref[...] = (xv * _rms_scale(xv) * w_ref[...]).astype(BF16)

        if exchange is not None:
            @pl.when(pl.program_id(0) == S // TR - 1)
            def _():
                exchange.finish(*exchange.split(ex_refs))

    res = pl.pallas_call(
        body, name="pre_norm", grid=(S // TR,), in_specs=[_row_spec(D), _vec_spec(D)] + ex_in_specs,
        out_specs=[_row_spec(D)] + ex_out_specs, out_shape=[jax.ShapeDtypeStruct((S, D), BF16)] + ex_out_shape,
        scratch_shapes=ex_scratch, compiler_params=_cparams(),
    )(x, w, *ex_in)
    return res[0], res[1:]


BQ = 512
NQ = S // BQ
LANE_BETA, LANE_G = 8, 12


def _gate_lanes(shape):
    lane = lax.broadcasted_iota(jnp.int32, shape, 1)
    return lane < LANE_BETA, (lane >= LANE_BETA) & (lane < LANE_G), (lane >= LANE_G) & (lane < LANE_G + NGH)


def _gates(proj, bias_vec, alog_vec):
    def body(s_ref, b_ref, a_ref, o_ref, carry_ref):
        i = pl.program_id(0)

        @pl.when(i == 0)
        def _():
            carry_ref[...] = jnp.zeros_like(carry_ref)

        z = s_ref[...] + b_ref[...]
        tail = jnp.log(1.0 + jnp.exp(-jnp.abs(z)))
        sp = jnp.maximum(z, 0.0) + tail
        lf = jnp.minimum(z, 0.0) - tail
        r = lax.broadcasted_iota(jnp.int32, (BQ, BQ), 0)
        c = lax.broadcasted_iota(jnp.int32, (BQ, BQ), 1)
        tri = (c <= r).astype(F32)
        cum = _hdot(tri, lf) + carry_ref[...]
        carry_ref[...] = cum[BQ - 1:BQ, :]
        is_fox, is_beta, is_g = _gate_lanes(z.shape)
        o_ref[...] = jnp.where(is_fox, cum, jnp.where(is_beta, _sigmoid(z), jnp.where(is_g, -jnp.exp(a_ref[...]) * sp, 0.0)))

    return pl.pallas_call(
        body, name="gates", grid=(NQ,),
        in_specs=[pl.BlockSpec((BQ, LANES), lambda i: (i, BLK_SMALL)), _vec_spec(LANES), _vec_spec(LANES)],
        out_specs=pl.BlockSpec((BQ, LANES), lambda i: (i, 0)), out_shape=jax.ShapeDtypeStruct((S, LANES), F32),
        scratch_shapes=[pltpu.VMEM((1, LANES), F32)], compiler_params=_cparams(),
    )(proj, bias_vec, alog_vec)


def _gates_bwd(proj, bias_vec, alog_vec, dgates_gdn, dcum_fox, dproj):
    def body(s_ref, b_ref, a_ref, dg_ref, dc_ref, dproj_in, dproj_ref, red_ref, carry_ref):
        del dproj_in
        i = pl.program_id(0)

        @pl.when(i == 0)
        def _():
            carry_ref[...] = jnp.zeros_like(carry_ref)
            red_ref[...] = jnp.zeros_like(red_ref)

        z = s_ref[...] + b_ref[...]
        dg = dg_ref[...] + dc_ref[...]
        r = lax.broadcasted_iota(jnp.int32, (BQ, BQ), 0)
        c = lax.broadcasted_iota(jnp.int32, (BQ, BQ), 1)
        upper = (c >= r).astype(F32)
        dlf = _hdot(upper, dg) + carry_ref[...]
        carry_ref[...] = dlf[0:1, :]
        sig = _sigmoid(z)
        g_scale = -jnp.exp(a_ref[...])
        is_fox, is_beta, is_g = _gate_lanes(z.shape)
        ds = jnp.where(is_fox, dlf * (1.0 - sig), jnp.where(is_beta, dg * sig * (1.0 - sig), jnp.where(is_g, dg * g_scale * sig, 0.0)))
        dproj_ref[:, 0:LANES] = ds.astype(BF16)
        dproj_ref[:, LANES:2 * LANES] = jnp.zeros((BQ, LANES), BF16)
        dalog = jnp.where(is_g, dg * g_scale * _softplus(z), 0.0)
        sums = jnp.sum(ds, axis=0, keepdims=True)
        red_ref[0:1, :] += jnp.where(is_fox[0:1], sums, 0.0)
        red_ref[1:2, :] += pltpu.roll(jnp.where(is_g[0:1], sums, 0.0), LANES - LANE_G, 1)
        red_ref[2:3, :] += pltpu.roll(jnp.sum(dalog, axis=0, keepdims=True), LANES - LANE_G, 1)

    blk = pl.BlockSpec((BQ, LANES), lambda i: (NQ - 1 - i, 0))
    return pl.pallas_call(
        body, name="gates_bwd", grid=(NQ,),
        in_specs=[pl.BlockSpec((BQ, LANES), lambda i: (NQ - 1 - i, BLK_SMALL)), _vec_spec(LANES), _vec_spec(LANES), blk, blk,
                  pl.BlockSpec(memory_space=pl.ANY)],
        out_specs=[pl.BlockSpec((BQ, 2 * LANES), lambda i: (NQ - 1 - i, BLK_SMALL // 2)), pl.BlockSpec((8, LANES), lambda i: (0, 0))],
        out_shape=[jax.ShapeDtypeStruct((S, DPROJ_PAD), BF16), jax.ShapeDtypeStruct((8, LANES), F32)],
        input_output_aliases={5: 0},
        scratch_shapes=[pltpu.VMEM((1, LANES), F32)], compiler_params=_cparams(),
    )(proj, bias_vec, alog_vec, dgates_gdn, dcum_fox, dproj)


FOX_SCALE = FHD ** -0.5
FOX_PAIRS = 2
FOX_PAIRS_BWD = 2


def _head_mask(e):
    lane = lax.broadcasted_iota(jnp.int32, (1, LANES), 1)
    return (lane >= e * FHD) & (lane < (e + 1) * FHD)


def _lane_col(vals, index):
    lane = lax.broadcasted_iota(jnp.int32, vals.shape, 1)
    return jnp.sum(jnp.where(lane == index, vals, 0.0), axis=1, keepdims=True)


def _sublane_row(vals, index):
    row = lax.broadcasted_iota(jnp.int32, vals.shape, 0)
    return jnp.sum(jnp.where(row == index, vals, 0.0), axis=0, keepdims=True)


def _pair_cols(c0, c1):
    lane = lax.broadcasted_iota(jnp.int32, (c0.shape[0], 2), 1)
    return jnp.where(lane == 0, c0, c1)


def _split3(x):
    hi = x.astype(BF16).astype(F32)
    rest = x - hi
    mid = rest.astype(BF16).astype(F32)
    return hi, mid, (rest - mid).astype(BF16).astype(F32)


def _fox_operand(vals, e, cum, is_query):
    lane = lax.broadcasted_iota(jnp.int32, (1, LANES), 1)
    base = (1 - e) * FHD
    parts = _split3(cum)
    own = jnp.where(_head_mask(e), vals * FOX_SCALE if is_query else vals, 0.0)
    cum_at, ones_at = (base, base + 3) if is_query else (base + 3, base)
    sign = 1.0 if is_query else -1.0
    out = own + jnp.where((lane >= ones_at) & (lane < ones_at + 3), 1.0, 0.0)
    for i, part in enumerate(parts):
        out = out + jnp.where(lane == cum_at + i, sign * part, 0.0)
    return out.astype(BF16)


def _causal_block():
    return lax.broadcasted_iota(jnp.int32, (BQ, BQ), 1) <= lax.broadcasted_iota(jnp.int32, (BQ, BQ), 0)


def _head_rms(o, masks):
    o2 = o * o
    r = [lax.rsqrt(jnp.sum(jnp.where(mk, o2, 0.0), axis=1, keepdims=True) * (1.0 / FHD) + EPS) for mk in masks]
    return jnp.where(masks[0], r[0], r[1])


def _hosted(exchange):
    if exchange is None:
        return [], [], [], [], []
    return (exchange.inputs, [HBM] * len(exchange.inputs), [HBM] * len(exchange.out_shape), exchange.out_shape,
            exchange.sem_shapes())


def _fox_fwd(proj, gates, w2, exchange=None):
    ex_in, ex_in_specs, ex_out_specs, ex_out_shape, ex_scratch = _hosted(exchange)

    n_in = 3 * FOX_PAIRS + 2
    heads = [(pp, e) for pp in range(FOX_PAIRS) for e in range(2)]

    def body(*refs):
        qkv_refs, g_ref, w_ref = refs[:3 * FOX_PAIRS], refs[3 * FOX_PAIRS], refs[3 * FOX_PAIRS + 1]
        mix_ref, o_ref, lse_ref = refs[n_in + len(ex_in):n_in + 3 + len(ex_in)]
        ka_ref, vb_ref = refs[n_in + 3 + len(ex_in) + len(ex_out_shape):n_in + 5 + len(ex_in) + len(ex_out_shape)]
        ex_refs = refs[n_in:n_in + len(ex_in)] + refs[n_in + 3 + len(ex_in):n_in + 3 + len(ex_in) + len(ex_out_shape)] + refs[-2:]
        grp, qi = pl.program_id(0), pl.program_id(1)

        def head_index(pp, e):
            return 2 * (FOX_PAIRS * grp + pp) + e

        if exchange is not None:
            @pl.when((grp == 0) & (qi == 0))
            def _():
                exchange.start(*exchange.split(ex_refs))

        @pl.when(qi == 0)
        def _():
            gt = g_ref[...]
            for pp in range(FOX_PAIRS):
                kv = qkv_refs[3 * pp + 1][...]
                for e in range(2):
                    ka_ref[2 * pp + e] = _fox_operand(kv, e, _lane_col(gt, head_index(pp, e)), False)
                vb_ref[pp] = qkv_refs[3 * pp + 2][...].astype(BF16)

        masks = [_head_mask(0), _head_mask(1)]
        gt = g_ref[pl.ds(pl.multiple_of(qi * BQ, BQ), BQ), :]
        qs = [_fox_operand(qkv_refs[3 * pp][...], e, _lane_col(gt, head_index(pp, e)), True) for pp, e in heads]
        n = range(len(heads))

        def block(kj, carry, diagonal):
            rows = pl.ds(pl.multiple_of(kj * BQ, BQ), BQ)
            s = [_dot(qs[i], ka_ref[i, rows, :], 1, 1) for i in n]
            if diagonal:
                s = [jnp.where(_causal_block(), s[i], -jnp.inf) for i in n]
            m_new = [jnp.maximum(carry[i][0], jnp.max(s[i], axis=-1, keepdims=True)) for i in n]
            p = [jnp.exp(s[i] - m_new[i]) for i in n]
            alpha = [jnp.exp(carry[i][0] - m_new[i]) for i in n]
            l_new = [alpha[i] * carry[i][1] + jnp.sum(p[i], axis=-1, keepdims=True) for i in n]
            pv = [_dot(p[i], vb_ref[heads[i][0], rows, :]) for i in n]
            return tuple((m_new[i], l_new[i], alpha[i] * carry[i][2] + pv[i]) for i in n)

        one = (jnp.full((BQ, 1), -jnp.inf, F32), jnp.zeros((BQ, 1), F32), jnp.zeros((BQ, LANES), F32))
        below = lax.fori_loop(0, qi, lambda kj, carry: block(kj, carry, False), (one,) * len(heads))
        done = block(qi, below, True)
        for pp in range(FOX_PAIRS):
            (m0, l0, a0), (m1, l1, a1) = done[2 * pp], done[2 * pp + 1]
            o = jnp.where(masks[0], a0 / l0, a1 / l1)
            cols = slice(pp * LANES, (pp + 1) * LANES)
            o_ref[:, cols] = o
            mix_ref[:, cols] = (o * _head_rms(o, masks) * w_ref[...]).astype(BF16)
            lse_ref[pp] = _pair_cols(m0 + jnp.log(l0), m1 + jnp.log(l1))

        if exchange is not None:
            @pl.when((grp == NPAIR // FOX_PAIRS // 2) & (qi == 0))
            def _():
                exchange.middle(*exchange.split(ex_refs))

            @pl.when((grp == NPAIR // FOX_PAIRS - 1) & (qi == NQ - 1))
            def _():
                exchange.rest(*exchange.split(ex_refs))

    qkv_specs = []
    for pp in range(FOX_PAIRS):
        qkv_specs.append(pl.BlockSpec((BQ, LANES), lambda g, i, pp=pp: (i, 3 * (FOX_PAIRS * g + pp))))
        qkv_specs.append(pl.BlockSpec((S, LANES), lambda g, i, pp=pp: (0, 3 * (FOX_PAIRS * g + pp) + 1)))
        qkv_specs.append(pl.BlockSpec((S, LANES), lambda g, i, pp=pp: (0, 3 * (FOX_PAIRS * g + pp) + 2)))
    blk = pl.BlockSpec((BQ, FOX_PAIRS * LANES), lambda g, i: (i, g))
    res = pl.pallas_call(
        body, name="fox_fwd", grid=(NPAIR // FOX_PAIRS, NQ),
        in_specs=qkv_specs + [pl.BlockSpec((S, LANES), lambda g, i: (0, 0)), pl.BlockSpec((1, LANES), lambda g, i: (0, 0))]
        + ex_in_specs,
        out_specs=[blk, blk, pl.BlockSpec((FOX_PAIRS, BQ, 2), lambda g, i: (g, i, 0))] + ex_out_specs,
        out_shape=[jax.ShapeDtypeStruct((S, D), BF16), jax.ShapeDtypeStruct((S, DFOX), F32),
                   jax.ShapeDtypeStruct((NPAIR, S, 2), F32)] + ex_out_shape,
        scratch_shapes=[pltpu.VMEM((2 * FOX_PAIRS, S, LANES), BF16), pltpu.VMEM((FOX_PAIRS, S, LANES), BF16)] + ex_scratch,
        compiler_params=_cparams(),
    )(*([proj] * (3 * FOX_PAIRS)), gates, w2, *ex_in)
    return res[0], res[1], res[2], res[3:]


def _fox_norm_bwd(o, dmix, w2, exchange=None):
    ex_in, ex_in_specs, ex_out_specs, ex_out_shape, ex_scratch = _hosted(exchange)

    def body(*refs):
        o_ref, g_ref, w_ref = refs[:3]
        do_ref, dl_ref, dw_ref = refs[3 + len(ex_in):6 + len(ex_in)]
        ex_refs = refs[3:3 + len(ex_in)] + refs[6 + len(ex_in):]
        hp, qi = pl.program_id(0), pl.program_id(1)

        if exchange is not None:
            @pl.when((hp == 0) & (qi == 0))
            def _():
                exchange.start(*exchange.split(ex_refs))

        masks = [_head_mask(0), _head_mask(1)]
        ov = o_ref[...]
        g = g_ref[...]
        r = _head_rms(ov, masks)
        gw = g * w_ref[...]
        gwo = gw * ov
        mean = [jnp.sum(jnp.where(mk, gwo, 0.0), axis=1, keepdims=True) * (1.0 / FHD) for mk in masks]
        do = r * gw - ov * (r * r * r) * jnp.where(masks[0], mean[0], mean[1])
        do_ref[...] = do.astype(BF16)
        doo = do * ov
        dl_ref[...] = _pair_cols(*[jnp.sum(jnp.where(mk, doo, 0.0), axis=1, keepdims=True) for mk in masks])

        @pl.when((hp == 0) & (qi == 0))
        def _():
            dw_ref[...] = jnp.zeros_like(dw_ref)

        dw_ref[...] += jnp.sum(g * ov * r, axis=0, keepdims=True)

        @pl.when((hp == NPAIR - 1) & (qi == NQ - 1))
        def _():
            dw = dw_ref[...]
            dw_ref[...] = dw + pltpu.roll(dw, FHD, 1)
            if exchange is not None:
                exchange.finish(*exchange.split(ex_refs))

    blk = pl.BlockSpec((BQ, LANES), lambda hp, i: (i, hp))
    vec = pl.BlockSpec((1, LANES), lambda hp, i: (0, 0))
    res = pl.pallas_call(
        body, name="fox_norm_bwd", grid=(NPAIR, NQ), in_specs=[blk, blk, vec] + ex_in_specs,
        out_specs=[blk, pl.BlockSpec((None, BQ, 2), lambda hp, i: (hp, i, 0)), vec] + ex_out_specs,
        out_shape=[jax.ShapeDtypeStruct((S, DFOX), BF16), jax.ShapeDtypeStruct((NPAIR, S, 2), F32),
                   jax.ShapeDtypeStruct((1, LANES), F32)] + ex_out_shape,
        scratch_shapes=ex_scratch, compiler_params=_cparams(),
    )(o, dmix, w2, *ex_in)
    return res[0], res[1], res[2], res[3:]


def _fox_bwd(proj, do, gates, lse, delta, exchange=None):
    ex_in, ex_in_specs, ex_out_specs, ex_out_shape, ex_scratch = _hosted(exchange)

    pg = FOX_PAIRS_BWD
    n_in = 3 * pg + 4
    heads = [(pp, e) for pp in range(pg) for e in range(2)]

    def body(*refs):
        qkv_refs = refs[:3 * pg]
        do_ref, g_ref, lse_ref, dl_ref = refs[3 * pg:n_in]
        dproj_ref, dc_ref = refs[n_in + len(ex_in):n_in + 2 + len(ex_in)]
        qa_ref, dq_ref = refs[n_in + 2 + len(ex_in) + len(ex_out_shape):n_in + 4 + len(ex_in) + len(ex_out_shape)]
        ex_refs = refs[n_in:n_in + len(ex_in)] + refs[n_in + 2 + len(ex_in):n_in + 2 + len(ex_in) + len(ex_out_shape)] + refs[-2:]
        grp, kj = pl.program_id(0), pl.program_id(1)

        def head_index(pp, e):
            return 2 * (pg * grp + pp) + e

        if exchange is not None:
            @pl.when((grp == 0) & (kj == 0))
            def _():
                exchange.start(*exchange.split(ex_refs))

        @pl.when(kj == 0)
        def _():
            gt = g_ref[...]
            for pp in range(pg):
                qv = qkv_refs[3 * pp][...]
                for e in range(2):
                    qa_ref[2 * pp + e] = _fox_operand(qv, e, _lane_col(gt, head_index(pp, e)), True)
            dq_ref[...] = jnp.zeros_like(dq_ref)

        @pl.when((grp == 0) & (kj == 0))
        def _():
            dc_ref[...] = jnp.zeros_like(dc_ref)

        masks = [_head_mask(0), _head_mask(1)]
        krows = pl.ds(pl.multiple_of(kj * BQ, BQ), BQ)
        gk = g_ref[krows, :]
        kas = [_fox_operand(qkv_refs[3 * pp + 1][...], e, _lane_col(gk, head_index(pp, e)), False) for pp, e in heads]
        vbs = [qkv_refs[3 * pp + 2][...].astype(BF16) for pp in range(pg)]
        lane = lax.broadcasted_iota(jnp.int32, (BQ, LANES), 1)
        n = range(len(heads))

        def block(qi, carry, diagonal):
            dks, dvs, css = carry
            rows = pl.ds(pl.multiple_of(qi * BQ, BQ), BQ)
            qa = [qa_ref[i, rows, :] for i in n]
            s = [_dot(qa[i], kas[i], 1, 1) for i in n]
            if diagonal:
                s = [jnp.where(_causal_block(), s[i], -jnp.inf) for i in n]
            dov = [do_ref[rows, pp * LANES:(pp + 1) * LANES] for pp in range(pg)]
            doe = [jnp.where(masks[e], dov[pp], jnp.zeros_like(dov[pp])) for pp, e in heads]
            lse2 = [lse_ref[pp, rows, :] for pp in range(pg)]
            dl2 = [dl_ref[pp, rows, :] for pp in range(pg)]
            p = [jnp.exp(s[i] - _lane_col(lse2[heads[i][0]], heads[i][1])) for i in n]
            dp = [_dot(doe[i], vbs[heads[i][0]], 1, 1) for i in n]
            ds = [p[i] * (dp[i] - _lane_col(dl2[heads[i][0]], heads[i][1])) for i in n]
            dv_part = [_dot(p[i], doe[i], 0, 0) for i in n]
            dk_part = [_dot(ds[i], jnp.where(masks[heads[i][1]], qa[i], jnp.zeros_like(qa[i])), 0, 0) for i in n]
            dq_part = [jnp.where(masks[heads[i][1]], _dot(ds[i], kas[i]), 0.0) for i in n]
            css = tuple(css[i] + jnp.sum(ds[i], axis=0, keepdims=True) for i in n)
            dc = jnp.zeros((BQ, LANES), F32)
            for i in n:
                dc = dc + jnp.where(lane == head_index(*heads[i]), jnp.sum(ds[i], axis=1, keepdims=True), 0.0)
            for pp in range(pg):
                dq_ref[pp, rows, :] += (dq_part[2 * pp] + dq_part[2 * pp + 1]) * FOX_SCALE
            dc_ref[rows, :] += dc
            dks = tuple(dks[pp] + dk_part[2 * pp] + dk_part[2 * pp + 1] for pp in range(pg))
            dvs = tuple(dvs[pp] + dv_part[2 * pp] + dv_part[2 * pp + 1] for pp in range(pg))
            return dks, dvs, css

        zero = jnp.zeros((BQ, LANES), F32)
        first = block(kj, ((zero,) * pg, (zero,) * pg, (jnp.zeros((1, BQ), F32),) * len(heads)), True)
        dks, dvs, css = lax.fori_loop(kj + 1, NQ, lambda qi, carry: block(qi, carry, False), first)
        r = lax.broadcasted_iota(jnp.int32, (BQ, BQ), 0)
        c = lax.broadcasted_iota(jnp.int32, (BQ, BQ), 1)
        dcol = jnp.zeros((BQ, LANES), F32)
        for i in n:
            col = jnp.sum(jnp.where(r == c, css[i], 0.0), axis=1, keepdims=True)
            dcol = dcol + jnp.where(lane == head_index(*heads[i]), col, 0.0)
        dc_ref[krows, :] -= dcol
        for pp in range(pg):
            base = 3 * pp * LANES
            dproj_ref[krows, base + LANES:base + 2 * LANES] = dks[pp].astype(BF16)
            dproj_ref[krows, base + 2 * LANES:base + 3 * LANES] = dvs[pp].astype(BF16)

        @pl.when(kj == NQ - 1)
        def _():
            for pp in range(pg):
                dproj_ref[:, 3 * pp * LANES:(3 * pp + 1) * LANES] = dq_ref[pp].astype(BF16)

        if exchange is not None:
            @pl.when((grp == NPAIR // pg // 2) & (kj == 0))
            def _():
                exchange.middle(*exchange.split(ex_refs))

            @pl.when((grp == NPAIR // pg - 1) & (kj == NQ - 1))
            def _():
                exchange.rest(*exchange.split(ex_refs))

    qkv_specs = []
    for pp in range(pg):
        qkv_specs.append(pl.BlockSpec((S, LANES), lambda g, j, pp=pp: (0, 3 * (pg * g + pp))))
        qkv_specs.append(pl.BlockSpec((BQ, LANES), lambda g, j, pp=pp: (j, 3 * (pg * g + pp) + 1)))
        qkv_specs.append(pl.BlockSpec((BQ, LANES), lambda g, j, pp=pp: (j, 3 * (pg * g + pp) + 2)))
    pair = pl.BlockSpec((pg, S, 2), lambda g, j: (g, 0, 0))
    res = pl.pallas_call(
        body, name="fox_bwd", grid=(NPAIR // pg, NQ),
        in_specs=qkv_specs + [pl.BlockSpec((S, pg * LANES), lambda g, j: (0, g)), pl.BlockSpec((S, LANES), lambda g, j: (0, 0)),
                              pair, pair] + ex_in_specs,
        out_specs=[pl.BlockSpec((S, 3 * pg * LANES), lambda g, j: (0, g)), pl.BlockSpec((S, LANES), lambda g, j: (0, 0))]
        + ex_out_specs,
        out_shape=[jax.ShapeDtypeStruct((S, DPROJ_PAD), BF16), jax.ShapeDtypeStruct((S, LANES), F32)] + ex_out_shape,
        scratch_shapes=[pltpu.VMEM((2 * pg, S, LANES), BF16), pltpu.VMEM((pg, S, LANES), F32)] + ex_scratch,
        compiler_params=_cparams(),
    )(*([proj] * (3 * pg)), do, gates, lse, delta, *ex_in)
    return res[0], res[1], res[2:]


NQKV = 3 * NGH
GDN_QSCALE = GHD ** -0.5


def _shift_down(x, s):
    if s == 0:
        return x
    row = lax.broadcasted_iota(jnp.int32, x.shape, 0)
    return jnp.where(row >= s, pltpu.roll(x, s, 0), 0.0)


def _shift_up(x, s):
    if s == 0:
        return x
    n = x.shape[0]
    row = lax.broadcasted_iota(jnp.int32, x.shape, 0)
    return jnp.where(row < n - s, pltpu.roll(x, n - s, 0), 0.0)


def _conv_taps(xv):
    return [_shift_down(xv, CONV_K - 1 - j) for j in range(CONV_K)]


def _conv_pre(taps, wv):
    pre = taps[CONV_K - 1] * wv[CONV_K - 1:CONV_K, :]
    for j in range(CONV_K - 1):
        pre = pre + taps[j] * wv[j:j + 1, :]
    return pre


def _l2_factors(b):
    return b < 2 * NGH, jnp.where(b < NGH, GDN_QSCALE, 1.0)


def _gdn_pre(proj, conv_w):
    def body(x_ref, w_ref, o_ref):
        b = pl.program_id(0)
        c = _silu(_conv_pre(_conv_taps(x_ref[...]), w_ref[...]))
        normed, scale = _l2_factors(b)
        rs = lax.rsqrt(jnp.sum(c * c, axis=-1, keepdims=True) + EPS)
        o_ref[...] = c * jnp.where(normed, rs, 1.0) * scale

    return pl.pallas_call(
        body, name="gdn_pre", grid=(NQKV,),
        in_specs=[pl.BlockSpec((S, GHD), lambda b: (0, BLK_GDN + b)), pl.BlockSpec((CONV_K, GHD), lambda b: (0, b))],
        out_specs=pl.BlockSpec((S, GHD), lambda b: (0, b)),
        out_shape=jax.ShapeDtypeStruct((S, NQKV * GHD), F32), compiler_params=_cparams(),
    )(proj, conv_w)


def _gdn_pre_bwd(proj, conv_w, dqkv, dproj):
    def body(x_ref, w_ref, dy_ref, dproj_in, dx_ref, dw_ref):
        del dproj_in
        b = pl.program_id(0)
        taps = _conv_taps(x_ref[...])
        wv = w_ref[...]
        pre = _conv_pre(taps, wv)
        sig = _sigmoid(pre)
        c = pre * sig
        normed, scale = _l2_factors(b)
        g = dy_ref[...] * scale
        rs = lax.rsqrt(jnp.sum(c * c, axis=-1, keepdims=True) + EPS)
        dc_n = rs * g - c * (rs * rs * rs) * jnp.sum(g * c, axis=-1, keepdims=True)
        dc = jnp.where(normed, dc_n, g)
        dpre = dc * sig * (1.0 + pre * (1.0 - sig))
        dx = dpre * wv[CONV_K - 1:CONV_K, :]
        for j in range(CONV_K - 1):
            dx = dx + _shift_up(dpre, CONV_K - 1 - j) * wv[j:j + 1, :]
        dx_ref[...] = dx.astype(BF16)
        for j in range(CONV_K):
            dw_ref[j:j + 1, :] = jnp.sum(dpre * taps[j], axis=0, keepdims=True)

    return pl.pallas_call(
        body, name="gdn_pre_bwd", grid=(NQKV,),
        in_specs=[pl.BlockSpec((S, GHD), lambda b: (0, BLK_GDN + b)), pl.BlockSpec((CONV_K, GHD), lambda b: (0, b)),
                  pl.BlockSpec((None, S, GHD), lambda b: (b // NGH, 0, b % NGH)), pl.BlockSpec(memory_space=pl.ANY)],
        out_specs=[pl.BlockSpec((S, GHD), lambda b: (0, BLK_GDN + b)), pl.BlockSpec((CONV_K, GHD), lambda b: (0, b))],
        out_shape=[jax.ShapeDtypeStruct((S, DPROJ_PAD), BF16), jax.ShapeDtypeStruct((CONV_K, NQKV * GHD), F32)],
        input_output_aliases={3: 0}, compiler_params=_cparams(),
    )(proj, conv_w, dqkv, dproj)


CB = 16
NCB = NCH // CB


def _chunk_prep(qs, ks, vs, gcols, bcols, t_saved=None):
    n = range(len(qs))
    r = lax.broadcasted_iota(jnp.int32, (CHUNK, CHUNK), 0)
    c = lax.broadcasted_iota(jnp.int32, (CHUNK, CHUNK), 1)
    incl = c <= r
    eye = (r == c).astype(F32)
    grow = [jnp.sum(gcols[i] * eye, axis=0, keepdims=True) for i in n]
    gc_col = [jnp.sum(jnp.where(incl, grow[i], 0.0), axis=1, keepdims=True) for i in n]
    gc_row = [jnp.sum(jnp.where(r <= c, gcols[i], 0.0), axis=0, keepdims=True) for i in n]
    decay = [jnp.exp(jnp.where(incl, gc_col[i] - gc_row[i], -jnp.inf)) for i in n]
    kb = [ks[i] * bcols[i] for i in n]
    vb = [vs[i] * bcols[i] for i in n]
    kk = [_mm_nt(kb[i], ks[i]) for i in n]
    m = [jnp.where(c < r, kk[i] * decay[i], 0.0) for i in n]
    if t_saved is None:
        t_inv = [eye - m[i] for i in n]
        p = [_dot3(m[i], m[i]) for i in n]
        for step in range(5):
            t_inv = [t_inv[i] + _dot3(t_inv[i], p[i]) for i in n]
            if step < 4:
                p = [_dot3(p[i], p[i]) for i in n]
    else:
        t_inv = [_saved_inverse(m[i], t_saved[i]) for i in n]
    egc = [jnp.exp(gc_col[i]) for i in n]
    u = [_mm_nn(t_inv[i], vb[i]) for i in n]
    w = [_mm_nn(t_inv[i], kb[i] * egc[i]) for i in n]
    qk = [_mm_nt(qs[i], ks[i]) for i in n]
    gc_last = [gc_col[i][CHUNK - 1:CHUNK, :] for i in n]
    return [(u[i], w[i], qk[i] * decay[i], qs[i] * egc[i], ks[i] * jnp.exp(gc_last[i] - gc_col[i]), jnp.exp(gc_last[i]),
             t_inv[i]) for i in n]


def _prep_specs():
    rows = CB * CHUNK
    qs = pl.BlockSpec((rows, GHD), lambda i, h: (i, h))
    ks = pl.BlockSpec((rows, GHD), lambda i, h: (i, NGH + h))
    vs = pl.BlockSpec((rows, GHD), lambda i, h: (i, 2 * NGH + h))
    gs = pl.BlockSpec((rows, LANES), lambda i, h: (i, 0))
    a_s = pl.BlockSpec((None, rows, CHUNK), lambda i, h: (h, i, 0))
    gl_s = pl.BlockSpec((None, CB, 1, LANES), lambda i, h: (h, i, 0, 0))
    return qs, ks, vs, gs, a_s, gl_s


def _gdn_prep(qkv, gates, exchange=None):
    ex_in, ex_in_specs, ex_out_specs, ex_out_shape, ex_scratch = _hosted(exchange)

    def body(*refs):
        q_ref, k_ref, v_ref, g_ref = refs[:4]
        u_ref, w_ref, qd_ref, kd_ref, a_ref, gl_ref, t_ref = refs[4 + len(ex_in):11 + len(ex_in)]
        ex_refs = refs[4:4 + len(ex_in)] + refs[11 + len(ex_in):]
        h = pl.program_id(1)

        if exchange is not None:
            @pl.when((pl.program_id(0) == 0) & (h == 0))
            def _():
                exchange.start(*exchange.split(ex_refs))

        chunks = [pl.ds(cidx * CHUNK, CHUNK) for cidx in range(CB)]
        gts = [g_ref[rows, :] for rows in chunks]
        outs = _chunk_prep([q_ref[rows, :] for rows in chunks], [k_ref[rows, :] for rows in chunks],
                           [v_ref[rows, :] for rows in chunks], [_lane_col(gt, LANE_G + h) for gt in gts],
                           [_lane_col(gt, LANE_BETA + h) for gt in gts])
        for cidx, rows in enumerate(chunks):
            u, w, a, qd, kd, gl, t_inv = outs[cidx]
            u_ref[rows, :] = u
            w_ref[rows, :] = w
            qd_ref[rows, :] = qd
            kd_ref[rows, :] = kd
            a_ref[rows, :] = a
            t_ref[rows, :] = t_inv
            gl_ref[cidx] = jnp.broadcast_to(gl, (1, LANES))

        if exchange is not None:
            @pl.when((pl.program_id(0) == NCB // 2) & (h == 0))
            def _():
                exchange.middle(*exchange.split(ex_refs))

            @pl.when((pl.program_id(0) == NCB - 1) & (h == NGH - 1))
            def _():
                exchange.rest(*exchange.split(ex_refs))

    qs, ks, vs, gs, a_s, gl_s = _prep_specs()
    tok = jax.ShapeDtypeStruct((S, DGDN), F32)
    sq = jax.ShapeDtypeStruct((NGH, S, CHUNK), F32)
    res = pl.pallas_call(
        body, name="gdn_prep", grid=(NCB, NGH), in_specs=[qs, ks, vs, gs] + ex_in_specs,
        out_specs=[qs, qs, qs, qs, a_s, gl_s, a_s] + ex_out_specs,
        out_shape=[tok, tok, tok, tok, sq, jax.ShapeDtypeStruct((NGH, NCH, 1, LANES), F32), sq] + ex_out_shape,
        scratch_shapes=ex_scratch, compiler_params=_cparams(),
    )(qkv, qkv, qkv, gates, *ex_in)
    return res[:7], res[7:]


def _gdn_prep_bwd(qkv, gates, t_inv, du, dw, dqd, dkd, da, dgl, exchange=None):
    ex_in, ex_in_specs, ex_out_specs, ex_out_shape, ex_scratch = _hosted(exchange)

    def body(*refs):
        q_ref, k_ref, v_ref, g_ref, t_ref, du_ref, dw_ref, dqd_ref, dkd_ref, da_ref, dgl_ref = refs[:11]
        dqkv_ref, dg_ref = refs[11 + len(ex_in):13 + len(ex_in)]
        ex_refs = refs[11:11 + len(ex_in)] + refs[13 + len(ex_in):]
        h = pl.program_id(1)

        if exchange is not None:
            @pl.when((pl.program_id(0) == 0) & (h == 0))
            def _():
                exchange.start(*exchange.split(ex_refs))

        @pl.when(h == 0)
        def _():
            dg_ref[...] = jnp.zeros_like(dg_ref)

        lane = lax.broadcasted_iota(jnp.int32, (CHUNK, LANES), 1)
        chunks = [pl.ds(cidx * CHUNK, CHUNK) for cidx in range(CB)]
        gts = [g_ref[rows, :] for rows in chunks]
        t_saved = [t_ref[rows, :] for rows in chunks]
        _, vjp = jax.vjp(lambda *args: [o[:6] for o in _chunk_prep(*args, t_saved=t_saved)],
                         [q_ref[rows, :] for rows in chunks], [k_ref[rows, :] for rows in chunks],
                         [v_ref[rows, :] for rows in chunks], [_lane_col(gt, LANE_G + h) for gt in gts],
                         [_lane_col(gt, LANE_BETA + h) for gt in gts])
        dqs, dks, dvs, dgcs, dbcs = vjp([(du_ref[rows, :], dw_ref[rows, :], da_ref[rows, :], dqd_ref[rows, :],
                                          dkd_ref[rows, :], dgl_ref[cidx][:, 0:1]) for cidx, rows in enumerate(chunks)])
        for cidx, rows in enumerate(chunks):
            dq, dk, dv, dgc, dbc = dqs[cidx], dks[cidx], dvs[cidx], dgcs[cidx], dbcs[cidx]
            dqkv_ref[0, rows, :] = dq
            dqkv_ref[1, rows, :] = dk
            dqkv_ref[2, rows, :] = dv
            dg_ref[rows, :] += jnp.where(lane == LANE_G + h, dgc, 0.0) + jnp.where(lane == LANE_BETA + h, dbc, 0.0)

        if exchange is not None:
            @pl.when((pl.program_id(0) == NCB - 1) & (h == NGH - 1))
            def _():
                exchange.finish(*exchange.split(ex_refs))

    qs, ks, vs, gs, a_s, gl_s = _prep_specs()
    res = pl.pallas_call(
        body, name="gdn_prep_bwd", grid=(NCB, NGH), in_specs=[qs, ks, vs, gs, a_s, qs, qs, qs, qs, a_s, gl_s] + ex_in_specs,
        out_specs=[pl.BlockSpec((3, CB * CHUNK, GHD), lambda i, h: (0, i, h)), gs] + ex_out_specs,
        out_shape=[jax.ShapeDtypeStruct((3, S, DGDN), F32), jax.ShapeDtypeStruct((S, LANES), F32)] + ex_out_shape,
        scratch_shapes=ex_scratch, compiler_params=_cparams(),
    )(qkv, qkv, qkv, gates, t_inv, du, dw, dqd, dkd, da, dgl, *ex_in)
    return res[0], res[1], res[2:]


def _scan_specs(nh, parts, reverse):
    wide, rows, chunks = nh * GHD, S // parts, NCH // parts

    def part(p):
        return parts - 1 - p if reverse else p

    hs = pl.BlockSpec((rows, wide), lambda g, p: (part(p), g))
    a_s = pl.BlockSpec((nh, rows, CHUNK), lambda g, p: (g, part(p), 0))
    gl_s = pl.BlockSpec((nh, chunks, 1, LANES), lambda g, p: (g, part(p), 0, 0))
    st_s = pl.BlockSpec((nh, chunks, GHD, GHD), lambda g, p: (g, part(p), 0, 0))
    gz_s = pl.BlockSpec((rows, wide), lambda g, p: (part(p), BLK_GZ // nh + g))
    mix_s = pl.BlockSpec((rows, wide), lambda g, p: (part(p), NPAIR // nh + g))
    return hs, a_s, gl_s, st_s, gz_s, mix_s


def _head_cols(hh):
    return slice(hh * GHD, (hh + 1) * GHD)


SCAN_HEADS, SCAN_PARTS = 4, 2
SCAN_HEADS_BWD, SCAN_PARTS_BWD = 4, 4


def _gdn_scan(u, w, qd, kd, a, gl, proj, w_norm, mix):
    heads = range(SCAN_HEADS)

    def body(u_ref, w_ref, qd_ref, kd_ref, a_ref, gl_ref, z_ref, wn_ref, mix_in, mix_ref, o_ref, st_ref, carry_ref):
        del mix_in

        @pl.when(pl.program_id(1) == 0)
        def _():
            carry_ref[...] = jnp.zeros_like(carry_ref)

        def step(ci, states):
            rows = pl.ds(pl.multiple_of(ci * CHUNK, CHUNK), CHUNK)
            for hh in heads:
                st_ref[hh, ci] = states[hh]
            ws = [_dot(w_ref[rows, _head_cols(hh)], states[hh]) for hh in heads]
            qs = [_dot(qd_ref[rows, _head_cols(hh)], states[hh]) for hh in heads]
            vn = [u_ref[rows, _head_cols(hh)] - ws[hh] for hh in heads]
            av = [_dot(a_ref[hh, rows, :], vn[hh]) for hh in heads]
            kv = [_dot(kd_ref[rows, _head_cols(hh)], vn[hh], 0, 0) for hh in heads]
            for hh in heads:
                o_ref[rows, _head_cols(hh)] = qs[hh] + av[hh]
            return tuple(states[hh] * gl_ref[hh, ci] + kv[hh] for hh in heads)

        last = lax.fori_loop(0, NCH // SCAN_PARTS, step, tuple(carry_ref[hh] for hh in heads))
        for hh in heads:
            carry_ref[hh] = last[hh]
            ov = o_ref[:, _head_cols(hh)]
            mix_ref[:, _head_cols(hh)] = (ov * _rms_scale(ov) * wn_ref[...] * _silu(z_ref[:, _head_cols(hh)])).astype(BF16)

    hs, a_s, gl_s, st_s, gz_s, mix_s = _scan_specs(SCAN_HEADS, SCAN_PARTS, False)
    return pl.pallas_call(
        body, name="gdn_scan", grid=(NGH // SCAN_HEADS, SCAN_PARTS),
        in_specs=[hs, hs, hs, hs, a_s, gl_s, gz_s, pl.BlockSpec((1, GHD), lambda g, p: (0, 0)),
                  pl.BlockSpec(memory_space=pl.ANY)],
        out_specs=[mix_s, hs, st_s],
        out_shape=[jax.ShapeDtypeStruct((S, D), BF16), jax.ShapeDtypeStruct((S, DGDN), F32),
                   jax.ShapeDtypeStruct((NGH, NCH, GHD, GHD), F32)],
        input_output_aliases={8: 0}, scratch_shapes=[pltpu.VMEM((SCAN_HEADS, GHD, GHD), F32)], compiler_params=_cparams(),
    )(u, w, qd, kd, a, gl, proj, w_norm, mix)


def _gdn_scan_bwd(dmix, o, proj, w_norm, u, w, qd, kd, a, gl, states, dproj, exchange=None):
    ex_in, ex_in_specs, ex_out_specs, ex_out_shape, ex_scratch = _hosted(exchange)
    groups = NGH // SCAN_HEADS_BWD

    def body(*refs):
        dy_ref, o_ref, z_ref, wn_ref, u_ref, w_ref, qd_ref, kd_ref, a_ref, gl_ref, st_ref = refs[:11]
        dz_ref, du_ref, dw_ref, dqd_ref, dkd_ref, da_ref, dgl_ref, dwn_ref = refs[12 + len(ex_in):20 + len(ex_in)]
        do_ref, carry_ref = refs[20 + len(ex_in) + len(ex_out_shape):22 + len(ex_in) + len(ex_out_shape)]
        ex_refs = refs[12:12 + len(ex_in)] + refs[20 + len(ex_in):20 + len(ex_in) + len(ex_out_shape)] + refs[-2:]
        heads = range(SCAN_HEADS_BWD)
        chunks = NCH // SCAN_PARTS_BWD

        if exchange is not None:
            @pl.when((pl.program_id(0) == 0) & (pl.program_id(1) == 0))
            def _():
                exchange.start(*exchange.split(ex_refs))

        @pl.when((pl.program_id(0) == 0) & (pl.program_id(1) == 0))
        def _():
            dwn_ref[...] = jnp.zeros_like(dwn_ref)

        @pl.when(pl.program_id(1) == 0)
        def _():
            carry_ref[...] = jnp.zeros_like(carry_ref)

        wn = wn_ref[...]
        for hh in heads:
            c = _head_cols(hh)
            ov = o_ref[:, c]
            zv = z_ref[:, c]
            g = dy_ref[:, c]
            sig = _sigmoid(zv)
            dz_ref[:, c] = (g * (ov * _rms_scale(ov) * wn) * sig * (1.0 + zv * (1.0 - sig))).astype(BF16)
            do, dwt = _rms_bwd(ov, wn, g * zv * sig)
            do_ref[:, c] = do
            dwn_ref[...] += jnp.sum(dwt, axis=0, keepdims=True)

        def step(t, dstates):
            ci = chunks - 1 - t
            rows = pl.ds(pl.multiple_of(ci * CHUNK, CHUNK), CHUNK)
            cols = [_head_cols(hh) for hh in heads]
            state = [st_ref[hh, ci] for hh in heads]
            dov = [do_ref[rows, cols[hh]] for hh in heads]
            wv = [w_ref[rows, cols[hh]] for hh in heads]
            ws = [_dot(wv[hh], state[hh]) for hh in heads]
            adov = [_dot(a_ref[hh, rows, :], dov[hh], 0, 0) for hh in heads]
            kds = [_dot(kd_ref[rows, cols[hh]], dstates[hh]) for hh in heads]
            dqd = [_dot(dov[hh], state[hh], 1, 1) for hh in heads]
            qdo = [_dot(qd_ref[rows, cols[hh]], dov[hh], 0, 0) for hh in heads]
            vn = [u_ref[rows, cols[hh]] - ws[hh] for hh in heads]
            dvn = [adov[hh] + kds[hh] for hh in heads]
            da = [_dot(dov[hh], vn[hh], 1, 1) for hh in heads]
            dkd = [_dot(vn[hh], dstates[hh], 1, 1) for hh in heads]
            dwv = [_dot(dvn[hh], state[hh], 1, 1) for hh in heads]
            wdv = [_dot(wv[hh], dvn[hh], 0, 0) for hh in heads]
            for hh in heads:
                da_ref[hh, rows, :] = da[hh]
                dqd_ref[rows, cols[hh]] = dqd[hh]
                dkd_ref[rows, cols[hh]] = dkd[hh]
                dgl = jnp.sum(jnp.sum(dstates[hh] * state[hh], axis=1, keepdims=True), axis=0, keepdims=True)
                dgl_ref[hh, ci] = jnp.broadcast_to(dgl, (1, LANES))
                du_ref[rows, cols[hh]] = dvn[hh]
                dw_ref[rows, cols[hh]] = -dwv[hh]
            return tuple(dstates[hh] * gl_ref[hh, ci] + qdo[hh] - wdv[hh] for hh in heads)

        last = lax.fori_loop(0, chunks, step, tuple(carry_ref[hh] for hh in heads))
        for hh in heads:
            carry_ref[hh] = last[hh]

        if exchange is not None:
            @pl.when((pl.program_id(0) == groups - 1) & (pl.program_id(1) == SCAN_PARTS_BWD - 1))
            def _():
                exchange.finish(*exchange.split(ex_refs))

    hs, a_s, gl_s, st_s, gz_s, mix_s = _scan_specs(SCAN_HEADS_BWD, SCAN_PARTS_BWD, True)
    vec = pl.BlockSpec((1, GHD), lambda g, p: (0, 0))
    tok = jax.ShapeDtypeStruct((S, DGDN), F32)
    res = pl.pallas_call(
        body, name="gdn_scan_bwd", grid=(groups, SCAN_PARTS_BWD),
        in_specs=[mix_s, hs, gz_s, vec, hs, hs, hs, hs, a_s, gl_s, st_s, pl.BlockSpec(memory_space=pl.ANY)] + ex_in_specs,
        out_specs=[gz_s, hs, hs, hs, hs, a_s, gl_s, vec] + ex_out_specs,
        out_shape=[jax.ShapeDtypeStruct((S, DPROJ_PAD), BF16), tok, tok, tok, tok,
                   jax.ShapeDtypeStruct((NGH, S, CHUNK), F32), jax.ShapeDtypeStruct((NGH, NCH, 1, LANES), F32),
                   jax.ShapeDtypeStruct((1, GHD), F32)] + ex_out_shape,
        input_output_aliases={11: 0},
        scratch_shapes=[pltpu.VMEM((S // SCAN_PARTS_BWD, SCAN_HEADS_BWD * GHD), F32),
                        pltpu.VMEM((SCAN_HEADS_BWD, GHD, GHD), F32)] + ex_scratch,
        compiler_params=_cparams(),
    )(dmix, o, proj, w_norm, u, w, qd, kd, a, gl, states, dproj, *ex_in)
    return res[:8], res[8:]


def _place():
    return lax.axis_index("x"), lax.axis_index("y"), lax.axis_index("c")


def _other_chips(x, y):
    return [(1 - x, y), (x, 1 - y), (1 - x, 1 - y)]


HBM = pl.BlockSpec(memory_space=pltpu.HBM)
VMEM = pl.BlockSpec(memory_space=pltpu.VMEM)


def _half_rows(ref_or_rows, half):
    rows = ref_or_rows // 2
    return pl.ds(pl.multiple_of(half * rows, rows), rows)


class _Exchange:
    def __init__(self, inputs, out_shape, n_sems, start, finish=None, middle=None, rest=None):
        self.inputs, self.out_shape, self.n_sems, self.start = inputs, out_shape, n_sems, start
        if finish is None:
            def finish(*refs):
                middle(*refs)
                rest(*refs)
        self.finish = finish
        self.middle = middle if middle is not None else (lambda *refs: None)
        self.rest = rest if rest is not None else finish

    def sem_shapes(self):
        return [pltpu.SemaphoreType.DMA((self.n_sems,)), pltpu.SemaphoreType.DMA((self.n_sems,))]

    def split(self, refs):
        n_in, n_out = len(self.inputs), len(self.out_shape)
        return refs[:n_in], refs[n_in:n_in + n_out], refs[n_in + n_out], refs[n_in + n_out + 1]


def _run_exchange(ex, name):
    def body(*refs):
        parts = ex.split(refs)
        ex.start(*parts)
        ex.finish(*parts)

    return pl.pallas_call(
        body, name=name, in_specs=[HBM] * len(ex.inputs), out_specs=[HBM] * len(ex.out_shape), out_shape=ex.out_shape,
        scratch_shapes=ex.sem_shapes(), compiler_params=_cparams(),
    )(*ex.inputs)


def _allgather_exchange(shards, whole=()):
    n, nw = len(shards), len(whole)
    slots = 8

    def plan(src, outs, send_sems, recv_sems):
        x, y, c = _place()
        via_x, via_y, diagonal = _other_chips(x, y)
        id_x, id_y, id_diagonal = [2 * chip[0] + chip[1] for chip in (via_x, via_y, diagonal)]
        me, sibling = (x, y, c), (x, y, 1 - c)

        def rows_of(a, half, quarter):
            total = src[a].shape[0]
            if quarter is None:
                return _half_rows(total, half)
            return pl.ds(pl.multiple_of(half * (total // 2) + quarter * (total // 4), total // 4), total // 4)

        def copy(a, k, chip_index, half, quarter, to, from_src=False):
            rows = rows_of(a, half, quarter)
            dst = outs[a].at[chip_index, rows]
            return pltpu.make_async_remote_copy(
                src_ref=src[a].at[rows] if from_src else dst, dst_ref=dst, send_sem=send_sems.at[slots * a + k],
                recv_sem=recv_sems.at[slots * a + k], device_id=to, device_id_type=MESH)

        def whole_copy(b, k, chip_index, to):
            return pltpu.make_async_remote_copy(
                src_ref=src[n + b], dst_ref=outs[n + b].at[chip_index], send_sem=send_sems.at[slots * n + 3 * b + k],
                recv_sem=recv_sems.at[slots * n + 3 * b + k], device_id=to, device_id_type=MESH)

        first, stages, last = [], [], []
        for a in range(n):
            first += [copy(a, 0, 2 * x + y, c, None, (*via_x, c), True), copy(a, 1, 2 * x + y, c, None, (*via_y, c), True)]
            stages.append([
                (copy(a, 0, id_x, c, None, me),
                 [copy(a, 2, id_x, c, 0, (*via_y, c)), copy(a, 4, id_x, c, None, sibling)]),
                (copy(a, 1, id_y, c, None, me),
                 [copy(a, 3, id_y, c, 1, (*via_x, c)), copy(a, 5, id_y, c, None, sibling)]),
                (copy(a, 2, id_diagonal, c, 0, me), [copy(a, 6, id_diagonal, c, 0, sibling)]),
                (copy(a, 3, id_diagonal, c, 1, me), [copy(a, 7, id_diagonal, c, 1, sibling)]),
            ])
            last += [copy(a, 4, id_x, 1 - c, None, me), copy(a, 5, id_y, 1 - c, None, me),
                     copy(a, 6, id_diagonal, 1 - c, 0, me), copy(a, 7, id_diagonal, 1 - c, 1, me)]
        for b in range(nw):
            for k, (chip, index) in enumerate(((via_x, id_x), (via_y, id_y), (diagonal, id_diagonal))):
                first.append(whole_copy(b, k, 2 * x + y, (*chip, c)))
                last.append(whole_copy(b, k, index, me))
        return first, stages, last

    def start(*refs):
        for cp in plan(*refs)[0]:
            cp.start()

    def pass_on(stages, which):
        for stage in which:
            for per_shard in stages:
                lands, onward = per_shard[stage]
                lands.wait_recv()
                for cp in onward:
                    cp.start()

    def middle(*refs):
        pass_on(plan(*refs)[1], (0, 1))

    def rest(*refs):
        first, stages, last = plan(*refs)
        pass_on(stages, (2, 3))
        for cp in last:
            cp.wait_recv()
        for cp in first + [cp for per_shard in stages for _, onward in per_shard for cp in onward]:
            cp.wait_send()

    out_shape = [jax.ShapeDtypeStruct((NCHIP,) + s.shape, s.dtype) for s in list(shards) + list(whole)]
    return _Exchange(list(shards) + list(whole), out_shape, slots * n + 3 * nw, start, middle=middle, rest=rest)


def _with_own(gathered, own):
    x, y, _ = _place()
    return lax.dynamic_update_index_in_dim(gathered, own, 2 * x + y, axis=0)


def _simple_exchange(inputs, out_shape, copies_of):
    def start(*refs):
        for cp in copies_of(*refs):
            cp.start()

    def finish(*refs):
        for cp in copies_of(*refs):
            cp.wait()

    return _Exchange(list(inputs), out_shape, len(out_shape) * 3, start, finish)


def _pair_exchange(grads):
    def copies_of(src, outs, send_sems, recv_sems):
        x, y, c = _place()
        return [pltpu.make_async_remote_copy(
            src_ref=src[a].at[:, _half_rows(src[a].shape[1], 1 - c)], dst_ref=outs[a], send_sem=send_sems.at[a],
            recv_sem=recv_sems.at[a], device_id=(x, y, 1 - c), device_id_type=MESH) for a in range(len(src))]

    return _simple_exchange(
        grads, [jax.ShapeDtypeStruct((g.shape[0], g.shape[1] // 2, g.shape[2]), g.dtype) for g in grads], copies_of)


def _pair_sum(grads, theirs, name):
    n = len(grads)

    def body(*refs):
        south = lax.axis_index("c") == 0
        for a in range(n):
            g = refs[a][...]
            half = g.shape[0] // 2
            mine = jnp.where(south, g[:half], g[half:])
            refs[2 * n + a][...] = (mine.astype(F32) + refs[n + a][...].astype(F32)).astype(BF16)

    def specs(arrs):
        return [pl.BlockSpec((None,) + g.shape[1:], lambda j: (j, 0, 0)) for g in arrs]

    return pl.pallas_call(
        body, name=name, grid=(NCHIP,), in_specs=specs(grads) + specs(theirs), out_specs=specs(theirs),
        out_shape=[jax.ShapeDtypeStruct(g.shape, BF16) for g in theirs], compiler_params=_cparams(),
    )(*grads, *theirs)


def _chip_exchange(parts):
    def copies_of(src, outs, send_sems, recv_sems):
        x, y, c = _place()
        return [pltpu.make_async_remote_copy(
            src_ref=src[a].at[2 * chip[0] + chip[1]], dst_ref=outs[a].at[k], send_sem=send_sems.at[3 * a + k],
            recv_sem=recv_sems.at[3 * a + k], device_id=(*chip, c), device_id_type=MESH)
            for a in range(len(src)) for k, chip in enumerate(_other_chips(x, y))]

    return _simple_exchange(parts, [jax.ShapeDtypeStruct((NCHIP - 1,) + p.shape[1:], p.dtype) for p in parts], copies_of)


def _chip_sum(parts, received, exchange=None):
    n = len(parts)
    steps = 4
    ex_in, ex_in_specs, ex_out_specs, ex_out_shape, ex_scratch = _hosted(exchange)

    def body(*refs):
        ex_refs = refs[2 * n:2 * n + len(ex_in)] + refs[3 * n + len(ex_in):]
        if exchange is not None:
            @pl.when(pl.program_id(0) == 0)
            def _():
                exchange.start(*exchange.split(ex_refs))

        chip = 2 * lax.axis_index("x") + lax.axis_index("y")
        for a in range(n):
            p, r = refs[a], refs[n + a]
            own = jnp.where(chip == 0, p[0], jnp.where(chip == 1, p[1], jnp.where(chip == 2, p[2], p[3])))
            refs[2 * n + len(ex_in) + a][...] = ((own.astype(F32) + r[0].astype(F32)) + r[1].astype(F32)) + r[2].astype(F32)

        if exchange is not None:
            @pl.when(pl.program_id(0) == steps - 1)
            def _():
                exchange.finish(*exchange.split(ex_refs))

    def specs(arrs):
        return [pl.BlockSpec((g.shape[0], g.shape[1] // steps, g.shape[2]), lambda i: (0, i, 0)) for g in arrs]

    out_specs = [pl.BlockSpec((g.shape[1] // steps, g.shape[2]), lambda i: (i, 0)) for g in parts]
    res = pl.pallas_call(
        body, name="grads_chip_sum", grid=(steps,), in_specs=specs(parts) + specs(received) + ex_in_specs,
        out_specs=out_specs + ex_out_specs, out_shape=[jax.ShapeDtypeStruct(g.shape[1:], F32) for g in parts] + ex_out_shape,
        scratch_shapes=ex_scratch, compiler_params=_cparams(),
    )(*parts, *received, *ex_in)
    return res[:n], res[n:]


def _pair_share(halves):
    def copies_of(src, outs, send_sems, recv_sems):
        x, y, c = _place()
        return [pltpu.make_async_remote_copy(
            src_ref=src[a], dst_ref=outs[a], send_sem=send_sems.at[a], recv_sem=recv_sems.at[a],
            device_id=(x, y, 1 - c), device_id_type=MESH) for a in range(len(src))]

    return _simple_exchange(halves, [jax.ShapeDtypeStruct(h.shape, F32) for h in halves], copies_of)


def _adamw_math(w, g, m, v):
    nm = ADAM_B1 * m + (1.0 - ADAM_B1) * g
    nv = ADAM_B2 * v + (1.0 - ADAM_B2) * jnp.square(g)
    m_hat = nm / (1.0 - ADAM_B1 ** ADAM_STEP)
    v_hat = nv / (1.0 - ADAM_B2 ** ADAM_STEP)
    return -ADAM_LR * (m_hat / (jnp.sqrt(v_hat) + ADAM_EPS) + ADAM_WD * w), nm, nv


def _adamw_big(ws, g_mine, g_theirs, ms, vs, exchange=None):
    n = len(ws)
    steps = 8
    ex_in, ex_in_specs, ex_out_specs, ex_out_shape, ex_scratch = _hosted(exchange)

    def body(*refs):
        ex_refs = refs[5 * n:5 * n + len(ex_in)] + refs[9 * n + len(ex_in):]
        outs = refs[5 * n + len(ex_in):9 * n + len(ex_in)]
        if exchange is not None:
            @pl.when(pl.program_id(0) == 0)
            def _():
                exchange.start(*exchange.split(ex_refs))

        own_half = (pl.program_id(0) // (steps // 2)) == lax.axis_index("c")
        for a in range(n):
            g = jnp.where(own_half, refs[n + a][...], refs[2 * n + a][...])
            d, nm, nv = _adamw_math(refs[a][...], g, refs[3 * n + a][...], refs[4 * n + a][...])
            outs[a][...] = g
            outs[n + a][...] = d
            outs[2 * n + a][...] = nm
            outs[3 * n + a][...] = nv

        if exchange is not None:
            @pl.when(pl.program_id(0) == steps - 1)
            def _():
                exchange.finish(*exchange.split(ex_refs))

    specs = [pl.BlockSpec((w.shape[0] // steps, w.shape[1]), lambda i: (i, 0)) for w in ws]
    half_specs = [pl.BlockSpec((g.shape[0] // (steps // 2), g.shape[1]), lambda i: (i % (steps // 2), 0)) for g in g_mine]
    shapes = [jax.ShapeDtypeStruct(w.shape, F32) for w in ws]
    res = pl.pallas_call(
        body, name="adamw_big", grid=(steps,), in_specs=specs + half_specs * 2 + specs * 2 + ex_in_specs,
        out_specs=specs * 4 + ex_out_specs, out_shape=shapes * 4 + ex_out_shape, scratch_shapes=ex_scratch,
        compiler_params=_cparams(),
    )(*ws, *g_mine, *g_theirs, *ms, *vs, *ex_in)
    return res[:n], res[n:2 * n], res[2 * n:3 * n], res[3 * n:4 * n], res[4 * n:]


def _adamw_in(w, g_mine, g_theirs, m, v):
    half = D // 2

    def body(w_ref, gm_ref, gt_ref, m_ref, v_ref, g_out, d_out, nm_out, nv_out, g_ref):
        south = lax.axis_index("c") == 0
        g_ref[0:half, :] = jnp.where(south, gm_ref[...], gt_ref[...])
        g_ref[half:D, :] = jnp.where(south, gt_ref[...], gm_ref[...])
        g = g_ref[0:CW, :]
        d, nm, nv = _adamw_math(w_ref[...], g, m_ref[...], v_ref[...])
        g_out[...] = g
        d_out[...] = d
        nm_out[...] = nm
        nv_out[...] = nv

    spec = pl.BlockSpec((CW, LANES), lambda i: (0, i))
    half_spec = pl.BlockSpec((half, LANES), lambda i: (0, i))
    return pl.pallas_call(
        body, name="adamw_in", grid=(D // LANES,), in_specs=[spec, half_spec, half_spec, spec, spec], out_specs=[spec] * 4,
        out_shape=[jax.ShapeDtypeStruct((CW, D), F32)] * 4, scratch_shapes=[pltpu.VMEM((D, LANES), F32)],
        compiler_params=_cparams(),
    )(w, g_mine, g_theirs, m, v)


NORM_NAMES = ("pre_mix_norm", "post_mix_norm", "pre_mlp_norm", "post_mlp_norm")
SMALL_NAMES = NORM_NAMES + ("gdn_conv_w", "fox_f_bias", "gdn_dt_bias", "gdn_a_log", "fox_out_norm", "gdn_out_norm")
CONV_COLS = 3 * DGDN // NCHIP


def _small_gather(d_norms, d_conv, sums, d_fox_norm, d_gdn_norm, loss_row):
    n_arrays = 6
    n_remote = n_arrays * (NDEV - 1)

    def copies_of(src, outs, send_sems, recv_sems):
        x, y, c = _place()
        me = 4 * x + 2 * y + c

        def from_me(chip_index):
            cols = pl.ds(pl.multiple_of(chip_index * CONV_COLS, LANES), CONV_COLS)
            return [src[0], src[1].at[:, cols], src[2], src[3], src[4], src[5]]

        local = [pltpu.make_async_copy(s, outs[a].at[me], send_sems.at[n_remote + a]) for a, s in enumerate(from_me(2 * x + y))]
        remote = []
        for k in range(1, NDEV):
            px, py, pc = x ^ ((k >> 2) & 1), y ^ ((k >> 1) & 1), c ^ (k & 1)
            remote += [pltpu.make_async_remote_copy(
                src_ref=s, dst_ref=outs[a].at[me], send_sem=send_sems.at[n_arrays * (k - 1) + a],
                recv_sem=recv_sems.at[n_arrays * (k - 1) + a], device_id=(px, py, pc), device_id_type=MESH)
                for a, s in enumerate(from_me(2 * px + py))]
        return local + remote

    def start(*refs):
        for cp in copies_of(*refs):
            cp.start()

    def finish(*refs):
        for cp in copies_of(*refs):
            cp.wait()

    shapes = [(4, D), (CONV_K, CONV_COLS), (8, LANES), (1, LANES), (1, LANES), (1, LANES)]
    return _Exchange([d_norms, d_conv, sums, d_fox_norm, d_gdn_norm, loss_row],
                     [jax.ShapeDtypeStruct((NDEV,) + s, F32) for s in shapes], n_remote + n_arrays, start, finish)


def _small_adamw(gathered, ws, ms, vs):
    n = len(SMALL_NAMES)
    ng = len(gathered)

    def body(*refs):
        def total(buf):
            acc = buf[0]
            for i in range(1, NDEV):
                acc = acc + buf[i]
            return acc

        t_norms, t_conv, t_sums, t_fn, t_gn, t_loss = [total(r) for r in refs[:ng]]
        w_refs, m_refs, v_refs = refs[ng:ng + n], refs[ng + n:ng + 2 * n], refs[ng + 2 * n:ng + 3 * n]
        outs = refs[ng + 3 * n:]
        outs[4 * n][...] = t_loss
        grads = [t_norms[i:i + 1, :] for i in range(4)] + [
            t_conv, t_sums[0:1, 0:NFH], t_sums[1:2, 0:NGH], t_sums[2:3, 0:NGH], t_fn[:, 0:FHD], t_gn]
        for a in range(n):
            d, nm, nv = _adamw_math(w_refs[a][...], grads[a], m_refs[a][...], v_refs[a][...])
            outs[a][...] = grads[a]
            outs[n + a][...] = d
            outs[2 * n + a][...] = nm
            outs[3 * n + a][...] = nv

    def whole(arr):
        return pl.BlockSpec(arr.shape, lambda i: (0,) * arr.ndim)

    res = pl.pallas_call(
        body, name="small_adamw", grid=(1,), in_specs=[whole(t) for t in gathered] + [whole(w) for w in ws] * 3,
        out_specs=[whole(w) for w in ws] * 4 + [pl.BlockSpec((1, LANES), lambda i: (0, 0))],
        out_shape=[jax.ShapeDtypeStruct(w.shape, F32) for w in ws] * 4 + [jax.ShapeDtypeStruct((1, LANES), F32)],
        compiler_params=_cparams(),
    )(*gathered, *ws, *ms, *vs)
    return res[:n], res[n:2 * n], res[2 * n:3 * n], res[3 * n:4 * n], res[4 * n]


CW = DPROJ // NCHIP
PROJ_RUNS = tuple((part * DFOX + hp * LANES, part * DFOX + (hp + 1) * LANES, (3 * hp + part) * LANES)
                  for hp in range(NPAIR) for part in range(3)) + (
    (1536, 1544, BLK_SMALL * LANES), (1544, 3080, BLK_GDN * LANES), (3080, 3088, BLK_SMALL * LANES + 8),
    (3088, 3600, BLK_GZ * LANES))


def _proj_pieces():
    pieces = []
    for lo, hi, at in PROJ_RUNS:
        while lo < hi:
            j = lo // CW
            end = min(hi, (j + 1) * CW)
            pieces.append((j, lo - j * CW, at, end - lo))
            at, lo = at + end - lo, end
    return pieces


RT = 256


def _to_padded_rows(gathered):
    def body(src_ref, out_ref, blocks_ref, rows_ref):
        blocks_ref[...] = src_ref[...].astype(F32)
        rows_ref[...] = jnp.zeros_like(rows_ref)
        for j, start, at, n in _proj_pieces():
            rows_ref[at:at + n, :] = blocks_ref[j, start:start + n, :]
        out_ref[...] = rows_ref[...].astype(out_ref.dtype)

    return pl.pallas_call(
        body, name="proj_rows_in", grid=(D // RT,), in_specs=[pl.BlockSpec((NCHIP, D, RT), lambda i: (0, 0, i))],
        out_specs=pl.BlockSpec((DPROJ_PAD, RT), lambda i: (0, i)), out_shape=jax.ShapeDtypeStruct((DPROJ_PAD, D), gathered.dtype),
        scratch_shapes=[pltpu.VMEM((NCHIP, D, RT), F32), pltpu.VMEM((DPROJ_PAD, RT), F32)], compiler_params=_cparams(),
    )(gathered)


def _from_padded_rows(w):
    def body(src_ref, out_ref, rows_ref, blocks_ref):
        rows_ref[...] = src_ref[...].astype(F32)
        blocks_ref[...] = jnp.zeros_like(blocks_ref)
        for j, start, at, n in _proj_pieces():
            blocks_ref[j, start:start + n, :] = rows_ref[at:at + n, :]
        out_ref[...] = blocks_ref[...].astype(out_ref.dtype)

    return pl.pallas_call(
        body, name="proj_rows_out", grid=(D // RT,), in_specs=[pl.BlockSpec((DPROJ_PAD, RT), lambda i: (0, i))],
        out_specs=pl.BlockSpec((NCHIP, D, RT), lambda i: (0, 0, i)), out_shape=jax.ShapeDtypeStruct((NCHIP, D, D), w.dtype),
        scratch_shapes=[pltpu.VMEM((DPROJ_PAD, RT), F32), pltpu.VMEM((NCHIP, D, RT), F32)], compiler_params=_cparams(),
    )(w)


def _local_step(x, target, first_weights, late_weights, reduce_late, reduce_in, pre_mix_norm, fox_f_bias, fox_out_norm,
                gdn_a_log, gdn_dt_bias, gdn_out_norm, post_mix_norm, pre_mlp_norm, post_mlp_norm):
    bias_vec = jnp.zeros((1, LANES), F32).at[0, 0:NFH].set(fox_f_bias).at[0, LANE_G:LANE_G + NGH].set(gdn_dt_bias)
    alog_vec = jnp.zeros((1, LANES), F32).at[0, LANE_G:LANE_G + NGH].set(gdn_a_log)
    w2 = jnp.concatenate([fox_out_norm, fox_out_norm], axis=1)

    h, first = _pre_norm(x, pre_mix_norm, exchange=first_weights[0])
    win_p, conv_w = first_weights[1](first)
    proj = _matmul(h, win_p, tb=True, tm=2048, tn=768, tk=1024, name="mm_proj", exchange=late_weights[0])
    proj, late_a = proj if late_weights[0] is not None else (proj, [])
    gates = _gates(proj, bias_vec, alog_vec)
    mix, fox_o, lse, late_b = _fox_fwd(proj, gates, w2, exchange=late_weights[1])
    qkv = _gdn_pre(proj, conv_w)
    (u, w, qd, kd, a_intra, gl, t_inv), _ = _gdn_prep(qkv, gates)
    wout, wup3 = late_weights[3](late_a, late_b)
    mix, gdn_raw, states = _gdn_scan(u, w, qd, kd, a_intra, gl, proj, gdn_out_norm, mix)
    def post_mix(acc, xv, w_post, w_pre_mlp):
        x1v = xv + acc * _rms_scale(acc) * w_post
        return acc, x1v, x1v * _rms_scale(x1v) * w_pre_mlp

    mixed, x1, h2 = _matmul(mix, wout, tm=512, tn=D, tk=1024, out_dtypes=(F32, F32, BF16), name="mm_out",
                            extra=(x, post_mix_norm, pre_mlp_norm), epilogue=post_mix)

    def relu2(acc):
        r = jnp.maximum(acc, 0.0)
        return r, r * r

    up_act = _matmul(h2, wup3, b3=True, tm=1024, tn=1024, tk=1024, out_dtypes=(BF16, BF16), epilogue=relu2,
                     name="mm_up", exchange=late_weights[2])
    (up_relu, act), late_c = up_act if late_weights[2] is not None else (up_act, [])
    wdown = late_weights[4](late_c)
    def loss_head(acc, x1v, tv, w):
        err = x1v + acc * _rms_scale(acc) * w - tv
        dx2v = err * (1.0 / D)
        dyv, dwt = _rms_bwd(acc, w, dx2v)
        part = 0.5 * jnp.sum(jnp.mean(err * err, axis=-1, keepdims=True), axis=0, keepdims=True)
        return dx2v, dyv, jnp.sum(dwt, axis=0, keepdims=True), jnp.broadcast_to(part, (1, D))

    dx2, dy, d_post_mlp, loss_wide = _matmul(
        act, wdown, tm=512, tn=D, tk=DFF, out_dtypes=(F32, BF16, F32, F32), extra=(x1, target, post_mlp_norm),
        epilogue=loss_head, n_sums=2, name="mm_down")
    loss_row = loss_wide[:, :LANES]

    dwdown = _matmul(act, dy, ta=True, tm=1024, tn=1024, tk=2048, out_dtypes=(BF16,), name="mm_dwdown")

    def relu2_bwd(acc, r):
        return (acc * 2.0 * r.astype(F32),)

    dup = _matmul(dy, wdown, tb=True, tm=1024, tn=1024, tk=1024, out_dtypes=(BF16,), extra=(up_relu,), epilogue=relu2_bwd,
                  name="mm_dact")
    dwup3 = _matmul(h2, dup, ta=True, tm=1024, tn=1024, tk=2048, out_dtypes=(BF16,), o3=True, name="mm_dwup")
    def mid_bwd(acc, x1v, dx2v, mixedv, w_pre_mlp, w_post):
        dxa, dwm = _rms_bwd(x1v, w_pre_mlp, acc)
        dx1v = dx2v + dxa
        dm, dwp = _rms_bwd(mixedv, w_post, dx1v)
        return dx1v, dm, jnp.sum(dwm, axis=0, keepdims=True), jnp.sum(dwp, axis=0, keepdims=True)

    dx1, dmixed, d_pre_mlp, d_post_mix = _matmul(
        dup, wup3, tb=True, b3=True, tm=512, tn=D, tk=DFF, out_dtypes=(F32, BF16, F32, F32),
        extra=(x1, dx2, mixed, pre_mlp_norm, post_mix_norm), epilogue=mid_bwd, n_sums=2, name="mm_dh2")
    dwout = _matmul(mix, dmixed, ta=True, tm=1024, tn=1024, tk=2048, out_dtypes=(BF16,), name="mm_dwout")
    dmix = _matmul(dmixed, wout, tb=True, tm=2048, tk=1024, name="mm_dmix")

    dfox, delta, d_fox_norm, from_sibling = _fox_norm_bwd(fox_o, dmix, w2, exchange=reduce_late[0](dwout, dwup3, dwdown))
    dproj, dcum_fox, reduced_a = _fox_bwd(proj, dfox, gates, lse, delta, exchange=reduce_late[1](from_sibling))
    (dproj, du, dw, dqd, dkd, da, dgl, d_gdn_norm), reduced_b = _gdn_scan_bwd(
        dmix, gdn_raw, proj, gdn_out_norm, u, w, qd, kd, a_intra, gl, states, dproj, exchange=reduce_late[2]())
    dqkv, dgates_gdn, reduced_c = _gdn_prep_bwd(qkv, gates, t_inv, du, dw, dqd, dkd, da, dgl, exchange=reduce_late[3]())
    reduced_late = (reduced_a, reduced_b, reduced_c)
    dproj, d_conv = _gdn_pre_bwd(proj, conv_w, dqkv, dproj)
    dproj, sums = _gates_bwd(proj, bias_vec, alog_vec, dgates_gdn, dcum_fox, dproj)

    dwin_p = _matmul(dproj, h, ta=True, tm=1280, tn=1024, tk=2048, out_dtypes=(BF16,), name="mm_dwin")
    exchange_in = reduce_in(dwin_p)
    def pre_norm_bwd(acc, xv, dx1v, w):
        dxa, dwt = _rms_bwd(xv, w, acc)
        return dx1v + dxa, jnp.sum(dwt, axis=0, keepdims=True)

    last = _matmul(dproj, win_p, tm=512, tn=D, tk=DPROJ_PAD, out_dtypes=(F32, F32), extra=(x, dx1, pre_mix_norm),
                   epilogue=pre_norm_bwd, n_sums=1, name="mm_dh", exchange=exchange_in)
    (grad_x, d_pre_mix), reduced_in = last if exchange_in is not None else (last, [])

    d_norms = jnp.concatenate([d_pre_mix, d_post_mix, d_pre_mlp, d_post_mlp], axis=0)
    return grad_x, (d_norms, d_conv, sums, d_fox_norm, d_gdn_norm, loss_row), reduced_late, reduced_in


def kernel(x, pre_mix_norm, w_in, fox_f_bias, fox_out_norm, gdn_conv_w, gdn_a_log, gdn_dt_bias, gdn_out_norm, w_out, post_mix_norm, pre_mlp_norm, w_up, w_down, post_mlp_norm, loss_target, m_pre_mix_norm, m_w_in, m_fox_f_bias, m_fox_out_norm, m_gdn_conv_w, m_gdn_a_log, m_gdn_dt_bias, m_gdn_out_norm, m_w_out, m_post_mix_norm, m_pre_mlp_norm, m_w_up, m_w_down, m_post_mlp_norm, v_pre_mix_norm, v_w_in, v_fox_f_bias, v_fox_out_norm, v_gdn_conv_w, v_gdn_a_log, v_gdn_dt_bias, v_gdn_out_norm, v_w_out, v_post_mix_norm, v_pre_mlp_norm, v_w_up, v_w_down, v_post_mlp_norm):
    weights = dict(pre_mix_norm=pre_mix_norm, w_in=w_in, fox_f_bias=fox_f_bias, fox_out_norm=fox_out_norm, gdn_conv_w=gdn_conv_w,
                   gdn_a_log=gdn_a_log, gdn_dt_bias=gdn_dt_bias, gdn_out_norm=gdn_out_norm, w_out=w_out, post_mix_norm=post_mix_norm,
                   pre_mlp_norm=pre_mlp_norm, w_up=w_up, w_down=w_down, post_mlp_norm=post_mlp_norm)
    m_in = dict(pre_mix_norm=m_pre_mix_norm, w_in=m_w_in, fox_f_bias=m_fox_f_bias, fox_out_norm=m_fox_out_norm, gdn_conv_w=m_gdn_conv_w,
                gdn_a_log=m_gdn_a_log, gdn_dt_bias=m_gdn_dt_bias, gdn_out_norm=m_gdn_out_norm, w_out=m_w_out, post_mix_norm=m_post_mix_norm,
                pre_mlp_norm=m_pre_mlp_norm, w_up=m_w_up, w_down=m_w_down, post_mlp_norm=m_post_mlp_norm)
    v_in = dict(pre_mix_norm=v_pre_mix_norm, w_in=v_w_in, fox_f_bias=v_fox_f_bias, fox_out_norm=v_fox_out_norm, gdn_conv_w=v_gdn_conv_w,
                gdn_a_log=v_gdn_a_log, gdn_dt_bias=v_gdn_dt_bias, gdn_out_norm=v_gdn_out_norm, w_out=v_w_out, post_mix_norm=v_post_mix_norm,
                pre_mlp_norm=v_pre_mlp_norm, w_up=v_w_up, w_down=v_w_down, post_mlp_norm=v_post_mlp_norm)
    order_w = ("pre_mix_norm", "w_in", "fox_f_bias", "fox_out_norm", "gdn_conv_w", "gdn_a_log", "gdn_dt_bias", "gdn_out_norm", "w_out",
               "post_mix_norm", "pre_mlp_norm", "w_up", "w_down", "post_mlp_norm")
    big = ("w_in", "w_out", "w_up", "w_down")

    def row(v):
        return v if v.ndim == 2 else v.reshape(1, -1)

    win_shard = jnp.pad(w_in.T.astype(BF16), ((0, D - CW), (0, 0)))

    def resolve_first(gathered):
        win_g, conv_g = gathered
        return (_to_padded_rows(_with_own(win_g, win_shard)),
                _with_own(conv_g, gdn_conv_w).transpose(1, 0, 2).reshape(CONV_K, 3 * DGDN))

    late_shards = [weights[n].astype(BF16) for n in big[1:]]

    gathered_down = []

    def resolve_out_up(gathered_out, gathered_mlp):
        gathered_down.append(gathered_mlp[1])
        return _with_own(gathered_out[0], late_shards[0]).reshape(D, D), _with_own(gathered_mlp[0], late_shards[1])

    def resolve_down(_):
        return _with_own(gathered_down[0], late_shards[2]).reshape(DFF, D)

    pair_sums, late_blocks = {}, []

    def pair_summed(names, blocks, theirs):
        for n, s in zip(names, _pair_sum(blocks, theirs, "grads_pair_sum_" + names[0])):
            pair_sums[n] = s

    def late_pair_exchange(dwout, dwup3, dwdown):
        late_blocks.extend([dwout.reshape(NCHIP, D // NCHIP, D), dwup3, dwdown.reshape(NCHIP, DFF // NCHIP, D)])
        return _pair_exchange(late_blocks)

    def late_chip_exchange(theirs):
        pair_summed(big[1:], late_blocks, theirs)
        return _chip_exchange([pair_sums["w_up"], pair_sums["w_down"]])

    def reduce_in(dwin_p):
        blocks = [_from_padded_rows(dwin_p)]
        pair_summed(big[:1], blocks, _run_exchange(_pair_exchange(blocks), "grads_pair_exchange_w_in"))
        return _chip_exchange([pair_sums["w_in"]])

    grad_x, small, received_late, received_in = _local_step(
        x[0], loss_target[0], (_allgather_exchange([win_shard], whole=[gdn_conv_w]), resolve_first),
        (_allgather_exchange(late_shards[:1]), _allgather_exchange(late_shards[1:]), None, resolve_out_up, resolve_down),
        (late_pair_exchange, late_chip_exchange, lambda: None, lambda: _chip_exchange([pair_sums["w_out"]])),
        reduce_in, row(pre_mix_norm), fox_f_bias, row(fox_out_norm), gdn_a_log, gdn_dt_bias,
        row(gdn_out_norm), row(post_mix_norm), row(pre_mlp_norm), row(post_mlp_norm))
    received_mlp, _, received_out = received_late

    g_mine, small_gathered = _chip_sum(
        [pair_sums[n] for n in big], list(received_in[:1]) + list(received_out[:1]) + list(received_mlp[:2]),
        exchange=_small_gather(*small))
    g_theirs = _run_exchange(_pair_share(g_mine), "grads_pair_share")

    g_big, d_big, nm_big, nv_big, _ = _adamw_big(
        [weights[n] for n in big[1:]], g_mine[1:], g_theirs[1:], [m_in[n] for n in big[1:]], [v_in[n] for n in big[1:]])
    in_t = _adamw_in(w_in.T, g_mine[0], g_theirs[0], m_w_in.T, v_w_in.T)
    g_small, d_small, nm_small, nv_small, loss_total = _small_adamw(
        small_gathered, [row(weights[n]) for n in SMALL_NAMES], [row(m_in[n]) for n in SMALL_NAMES],
        [row(v_in[n]) for n in SMALL_NAMES])

    grads, delta, new_m, new_v = {}, {}, {}, {}
    grads["w_in"], delta["w_in"], new_m["w_in"], new_v["w_in"] = [t.T for t in in_t]
    for i, n in enumerate(big[1:]):
        grads[n], delta[n], new_m[n], new_v[n] = g_big[i], d_big[i], nm_big[i], nv_big[i]
    for i, n in enumerate(SMALL_NAMES):
        shape = weights[n].shape
        grads[n], delta[n], new_m[n], new_v[n] = (g_small[i].reshape(shape), d_small[i].reshape(shape),
                                                  nm_small[i].reshape(shape), nv_small[i].reshape(shape))
    return (loss_total[0, 0], grad_x[None], *[grads[n] for n in order_w], *[delta[n] for n in order_w], *[new_m[n] for n in order_w],
            *[new_v[n] for n in order_w])
```

```python
import jax
import jax.numpy as jnp
from jax import lax
from jax.experimental import pallas as pl
from jax.experimental.pallas import tpu as pltpu

F32 = jnp.float32
BF16 = jnp.bfloat16
MESH = pl.DeviceIdType.MESH

S = 2048
D = 1024
NFH, FHD = 8, 64
NPAIR = NFH // 2
NGH, GHD = 4, 128
DFOX = NFH * FHD
DGDN = NGH * GHD
CHUNK = 64
NCH = S // CHUNK
CONV_K = 4
DFF = 4 * D
EPS = 1e-6
DPROJ = 3600
LANES = 128
DPROJ_PAD = 3840
BLK_GDN = 12
BLK_GZ = 24
BLK_SMALL = 28
NCHIP = 4
NDEV = 8
VMEM_LIMIT = 56 * 1024 * 1024

ADAM_LR = 0.001
ADAM_B1 = 0.9
ADAM_B2 = 0.999
ADAM_EPS = 1e-08
ADAM_WD = 0.01
ADAM_STEP = 10


def _cparams(**kw):
    return pltpu.CompilerParams(vmem_limit_bytes=VMEM_LIMIT, **kw)


def _dn(ca, cb):
    return (((ca,), (cb,)), ((), ()))


def _dot(a, b, ca=1, cb=0):
    return lax.dot_general(a.astype(BF16), b.astype(BF16), _dn(ca, cb), preferred_element_type=F32)


def _hdot(a, b, ca=1, cb=0):
    return lax.dot_general(a.astype(F32), b.astype(F32), _dn(ca, cb), precision=lax.Precision.HIGHEST,
                           preferred_element_type=F32)


def _dot3(a, b, ca=1, cb=0):
    a_hi, b_hi = a.astype(BF16), b.astype(BF16)
    a_lo, b_lo = (a - a_hi.astype(F32)).astype(BF16), (b - b_hi.astype(F32)).astype(BF16)
    dn = _dn(ca, cb)
    return (lax.dot_general(a_hi, b_hi, dn, preferred_element_type=F32)
            + (lax.dot_general(a_hi, b_lo, dn, preferred_element_type=F32)
               + lax.dot_general(a_lo, b_hi, dn, preferred_element_type=F32)))


@jax.custom_vjp
def _mm_nn(a, b):
    return _dot(a, b, 1, 0)


def _mm_nn_fwd(a, b):
    return _dot(a, b, 1, 0), (a, b)


def _mm_nn_bwd(res, g):
    a, b = res
    return _dot(g, b, 1, 1), _dot(a, g, 0, 0)


_mm_nn.defvjp(_mm_nn_fwd, _mm_nn_bwd)


@jax.custom_vjp
def _mm_nt(a, b):
    return _dot(a, b, 1, 1)


def _mm_nt_fwd(a, b):
    return _dot(a, b, 1, 1), (a, b)


def _mm_nt_bwd(res, g):
    a, b = res
    return _dot(g, b, 1, 0), _dot(g, a, 0, 0)


_mm_nt.defvjp(_mm_nt_fwd, _mm_nt_bwd)


@jax.custom_vjp
def _saved_inverse(m, t_inv):
    del m
    return t_inv


def _saved_inverse_fwd(m, t_inv):
    del m
    return t_inv, t_inv


def _saved_inverse_bwd(t_inv, g):
    return -_dot3(_dot3(t_inv, g, 0, 0), t_inv, 1, 1), jnp.zeros_like(t_inv)


_saved_inverse.defvjp(_saved_inverse_fwd, _saved_inverse_bwd)


def _sigmoid(z):
    return 1.0 / (1.0 + jnp.exp(-z))


def _softplus(z):
    return jnp.maximum(z, 0.0) + jnp.log(1.0 + jnp.exp(-jnp.abs(z)))


def _silu(z):
    return z * _sigmoid(z)


def _rms_scale(x):
    return lax.rsqrt(jnp.mean(x * x, axis=-1, keepdims=True) + EPS)


def _rms_bwd(x, w, g):
    r = _rms_scale(x)
    gw = g * w
    dx = r * gw - x * (r * r * r) * jnp.mean(gw * x, axis=-1, keepdims=True)
    return dx, g * x * r


def _matmul(a, b, *, name, ta=False, tb=False, tm=512, tn=512, tk=512, out_dtypes=(F32,), b3=False, o3=False,
            extra=(), epilogue=None, exchange=None, n_sums=0):
    m, k = (a.shape[1], a.shape[0]) if ta else a.shape
    if b3:
        n = b.shape[1] if tb else b.shape[0] * b.shape[2]
        kb = b.shape[0] * b.shape[2] if tb else b.shape[1]
    else:
        n, kb = (b.shape[0], b.shape[1]) if tb else (b.shape[1], b.shape[0])
    assert kb == k, (name, kb, k)
    tm, tn, tk = min(tm, m), min(tn, n), min(tk, k)
    assert m % tm == 0 and n % tn == 0 and k % tk == 0, (name, m, n, k, tm, tn, tk)
    nk = k // tk
    whole_k_blocks = b3 and tb and not ta and nk == 1 and b.shape[0] > 1
    n_extra = len(extra)
    n_out = len(out_dtypes)
    grid = (m // tm, n // tn, nk)
    ex_in, ex_in_specs, ex_out_specs, ex_out_shape, ex_scratch = _hosted(exchange)

    def body(*refs):
        a_ref, b_ref = refs[0], refs[1]
        extra_refs = refs[2:2 + n_extra]
        first_out = 2 + n_extra + len(ex_in)
        out_refs = refs[first_out:first_out + n_out]
        ex_refs = refs[2 + n_extra:first_out] + refs[first_out + n_out:first_out + n_out + len(ex_out_shape)] + refs[-2:]
        step = [pl.program_id(d) for d in range(3)]

        if exchange is not None:
            @pl.when((step[0] == 0) & (step[1] == 0) & (step[2] == 0))
            def _():
                exchange.start(*exchange.split(ex_refs))

        def finish(acc):
            outs = (acc,) if epilogue is None else epilogue(acc, *[r[...] for r in extra_refs])
            for o_ref, val in zip(out_refs[:n_out - n_sums], outs):
                o_ref[...] = val.astype(o_ref.dtype)
            for o_ref, val in zip(out_refs[n_out - n_sums:], outs[n_out - n_sums:]):
                @pl.when(step[0] == 0)
                def _(o_ref=o_ref, val=val):
                    o_ref[...] = val

                @pl.when(step[0] > 0)
                def _(o_ref=o_ref, val=val):
                    o_ref[...] += val

        if whole_k_blocks:
            width = b.shape[2]
            part = _dot(a_ref[:, 0:width], b_ref[0], 1, 1)
            for blk in range(1, b.shape[0]):
                part = part + _dot(a_ref[:, blk * width:(blk + 1) * width], b_ref[blk], 1, 1)
        else:
            part = _dot(a_ref[...], b_ref[...], 0 if ta else 1, 1 if tb else 0)
        if nk == 1:
            finish(part)
        else:
            acc_ref = refs[first_out + n_out + len(ex_out_shape)]

            @pl.when(step[2] == 0)
            def _():
                acc_ref[...] = part

            @pl.when(step[2] > 0)
            def _():
                acc_ref[...] += part

            @pl.when(step[2] == nk - 1)
            def _():
                finish(acc_ref[...])

        if exchange is not None:
            flat = (step[0] * grid[1] + step[1]) * nk + step[2]
            total = grid[0] * grid[1] * nk

            @pl.when(flat == total // 2)
            def _():
                exchange.middle(*exchange.split(ex_refs))

            @pl.when(flat == total - 1)
            def _():
                exchange.rest(*exchange.split(ex_refs))

    a_spec = pl.BlockSpec((tk, tm), lambda i, j, kk: (kk, i)) if ta else pl.BlockSpec((tm, tk), lambda i, j, kk: (i, kk))
    if whole_k_blocks:
        b_spec = pl.BlockSpec((b.shape[0], tn, b.shape[2]), lambda i, j, kk: (0, j, 0))
    elif b3 and tb:
        assert b.shape[2] == tk
        b_spec = pl.BlockSpec((None, tn, tk), lambda i, j, kk: (kk, j, 0))
    elif b3:
        assert b.shape[2] == tn
        b_spec = pl.BlockSpec((None, tk, tn), lambda i, j, kk: (j, kk, 0))
    elif tb:
        b_spec = pl.BlockSpec((tn, tk), lambda i, j, kk: (j, kk))
    else:
        b_spec = pl.BlockSpec((tk, tn), lambda i, j, kk: (kk, j))
    tile = pl.BlockSpec((tm, tn), lambda i, j, kk: (i, j))
    out_specs = [tile] * n_out
    out_shape = [jax.ShapeDtypeStruct((m, n), dt) for dt in out_dtypes]
    if o3:
        out_specs[0] = pl.BlockSpec((None, tm, tn), lambda i, j, kk: (j, i, 0))
        out_shape[0] = jax.ShapeDtypeStruct((n // tn, m, tn), out_dtypes[0])
    assert n_sums == 0 or tn == n
    for r in range(n_out - n_sums, n_out):
        out_specs[r] = pl.BlockSpec((1, tn), lambda i, j, kk: (0, 0))
        out_shape[r] = jax.ShapeDtypeStruct((1, n), out_dtypes[r])
    res = pl.pallas_call(
        body, name=name, grid=grid,
        in_specs=[a_spec, b_spec] + [tile if e.shape[0] == m else pl.BlockSpec((1, tn), lambda i, j, kk: (0, j)) for e in extra]
        + ex_in_specs, out_specs=out_specs + ex_out_specs,
        out_shape=out_shape + ex_out_shape,
        scratch_shapes=([pltpu.VMEM((tm, tn), F32)] if nk > 1 else []) + ex_scratch,
        compiler_params=_cparams(),
    )(a, b, *extra, *ex_in)
    if exchange is not None:
        return (res[0] if n_out == 1 else res[:n_out]), res[n_out:]
    return res[0] if n_out == 1 else res


TR = 256


def _row_spec(cols):
    return pl.BlockSpec((TR, cols), lambda i: (i, 0))


def _vec_spec(cols):
    return pl.BlockSpec((1, cols), lambda i: (0, 0))


def _pre_norm(x, w, exchange=None):
    ex_in, ex_in_specs, ex_out_specs, ex_out_shape, ex_scratch = _hosted(exchange)

    def body(*refs):
        x_ref, w_ref, h_ref = refs[0], refs[1], refs[2 + len(ex_in)]
        ex_refs = refs[2:2 + len(ex_in)] + refs[3 + len(ex_in):]
        if exchange is not None:
            @pl.when(pl.program_id(0) == 0)
            def _():
                exchange.start(*exchange.split(ex_refs))

        xv = x_ref[...]
        h_ref[...] = (xv * _rms_scale(xv) * w_ref[...]).astype(BF16)

        if exchange is not None:
            @pl.when(pl.program_id(0) == S // TR - 1)
            def _():
                exchange.finish(*exchange.split(ex_refs))

    res = pl.pallas_call(
        body, name="pre_norm", grid=(S // TR,), in_specs=[_row_spec(D), _vec_spec(D)] + ex_in_specs,
        out_specs=[_row_spec(D)] + ex_out_specs, out_shape=[jax.ShapeDtypeStruct((S, D), BF16)] + ex_out_shape,
        scratch_shapes=ex_scratch, compiler_params=_cparams(),
    )(x, w, *ex_in)
    return res[0], res[1:]


def _pre_norm_bwd(dh, x, w, dx1):
    def body(dh_ref, x_ref, w_ref, dx1_ref, dx_ref, dw_ref):
        i = pl.program_id(0)
        dxa, dwt = _rms_bwd(x_ref[...], w_ref[...], dh_ref[...])
        dx_ref[...] = dx1_ref[...] + dxa

        @pl.when(i == 0)
        def _():
            dw_ref[...] = jnp.zeros_like(dw_ref)

        dw_ref[...] += jnp.sum(dwt, axis=0, keepdims=True)

    return pl.pallas_call(
        body, name="pre_norm_bwd", grid=(S // TR,),
        in_specs=[_row_spec(D), _row_spec(D), _vec_spec(D), _row_spec(D)], out_specs=[_row_spec(D), _vec_spec(D)],
        out_shape=[jax.ShapeDtypeStruct((S, D), F32), jax.ShapeDtypeStruct((1, D), F32)], compiler_params=_cparams(),
    )(dh, x, w, dx1)


BQ = 512
NQ = S // BQ
LANE_BETA, LANE_G = 8, 12


def _gate_lanes(shape):
    lane = lax.broadcasted_iota(jnp.int32, shape, 1)
    return lane < LANE_BETA, (lane >= LANE_BETA) & (lane < LANE_G), (lane >= LANE_G) & (lane < LANE_G + NGH)


def _gates(proj, bias_vec, alog_vec):
    def body(s_ref, b_ref, a_ref, o_ref, carry_ref):
        i = pl.program_id(0)

        @pl.when(i == 0)
        def _():
            carry_ref[...] = jnp.zeros_like(carry_ref)

        z = s_ref[...] + b_ref[...]
        tail = jnp.log(1.0 + jnp.exp(-jnp.abs(z)))
        sp = jnp.maximum(z, 0.0) + tail
        lf = jnp.minimum(z, 0.0) - tail
        r = lax.broadcasted_iota(jnp.int32, (BQ, BQ), 0)
        c = lax.broadcasted_iota(jnp.int32, (BQ, BQ), 1)
        tri = (c <= r).astype(F32)
        cum = _hdot(tri, lf) + carry_ref[...]
        carry_ref[...] = cum[BQ - 1:BQ, :]
        is_fox, is_beta, is_g = _gate_lanes(z.shape)
        o_ref[...] = jnp.where(is_fox, cum, jnp.where(is_beta, _sigmoid(z), jnp.where(is_g, -jnp.exp(a_ref[...]) * sp, 0.0)))

    return pl.pallas_call(
        body, name="gates", grid=(NQ,),
        in_specs=[pl.BlockSpec((BQ, LANES), lambda i: (i, BLK_SMALL)), _vec_spec(LANES), _vec_spec(LANES)],
        out_specs=pl.BlockSpec((BQ, LANES), lambda i: (i, 0)), out_shape=jax.ShapeDtypeStruct((S, LANES), F32),
        scratch_shapes=[pltpu.VMEM((1, LANES), F32)], compiler_params=_cparams(),
    )(proj, bias_vec, alog_vec)


def _gates_bwd(proj, bias_vec, alog_vec, dgates_gdn, dcum_fox, dproj):
    def body(s_ref, b_ref, a_ref, dg_ref, dc_ref, dproj_in, dproj_ref, red_ref, carry_ref):
        del dproj_in
        i = pl.program_id(0)

        @pl.when(i == 0)
        def _():
            carry_ref[...] = jnp.zeros_like(carry_ref)
            red_ref[...] = jnp.zeros_like(red_ref)

        z = s_ref[...] + b_ref[...]
        dg = dg_ref[...] + dc_ref[...]
        r = lax.broadcasted_iota(jnp.int32, (BQ, BQ), 0)
        c = lax.broadcasted_iota(jnp.int32, (BQ, BQ), 1)
        upper = (c >= r).astype(F32)
        dlf = _hdot(upper, dg) + carry_ref[...]
        carry_ref[...] = dlf[0:1, :]
        sig = _sigmoid(z)
        g_scale = -jnp.exp(a_ref[...])
        is_fox, is_beta, is_g = _gate_lanes(z.shape)
        ds = jnp.where(is_fox, dlf * (1.0 - sig), jnp.where(is_beta, dg * sig * (1.0 - sig), jnp.where(is_g, dg * g_scale * sig, 0.0)))
        dproj_ref[:, 0:LANES] = ds.astype(BF16)
        dproj_ref[:, LANES:2 * LANES] = jnp.zeros((BQ, LANES), BF16)
        dalog = jnp.where(is_g, dg * g_scale * _softplus(z), 0.0)
        sums = jnp.sum(ds, axis=0, keepdims=True)
        red_ref[0:1, :] += jnp.where(is_fox[0:1], sums, 0.0)
        red_ref[1:2, :] += pltpu.roll(jnp.where(is_g[0:1], sums, 0.0), LANES - LANE_G, 1)
        red_ref[2:3, :] += pltpu.roll(jnp.sum(dalog, axis=0, keepdims=True), LANES - LANE_G, 1)

    blk = pl.BlockSpec((BQ, LANES), lambda i: (NQ - 1 - i, 0))
    return pl.pallas_call(
        body, name="gates_bwd", grid=(NQ,),
        in_specs=[pl.BlockSpec((BQ, LANES), lambda i: (NQ - 1 - i, BLK_SMALL)), _vec_spec(LANES), _vec_spec(LANES), blk, blk,
                  pl.BlockSpec(memory_space=pl.ANY)],
        out_specs=[pl.BlockSpec((BQ, 2 * LANES), lambda i: (NQ - 1 - i, BLK_SMALL // 2)), pl.BlockSpec((8, LANES), lambda i: (0, 0))],
        out_shape=[jax.ShapeDtypeStruct((S, DPROJ_PAD), BF16), jax.ShapeDtypeStruct((8, LANES), F32)],
        input_output_aliases={5: 0},
        scratch_shapes=[pltpu.VMEM((1, LANES), F32)], compiler_params=_cparams(),
    )(proj, bias_vec, alog_vec, dgates_gdn, dcum_fox, dproj)


FOX_SCALE = FHD ** -0.5
FOX_PAIRS = 2
FOX_PAIRS_BWD = 2


def _head_mask(e):
    lane = lax.broadcasted_iota(jnp.int32, (1, LANES), 1)
    return (lane >= e * FHD) & (lane < (e + 1) * FHD)


def _lane_col(vals, index):
    lane = lax.broadcasted_iota(jnp.int32, vals.shape, 1)
    return jnp.sum(jnp.where(lane == index, vals, 0.0), axis=1, keepdims=True)


def _sublane_row(vals, index):
    row = lax.broadcasted_iota(jnp.int32, vals.shape, 0)
    return jnp.sum(jnp.where(row == index, vals, 0.0), axis=0, keepdims=True)


def _pair_cols(c0, c1):
    lane = lax.broadcasted_iota(jnp.int32, (c0.shape[0], 2), 1)
    return jnp.where(lane == 0, c0, c1)


def _split3(x):
    hi = x.astype(BF16).astype(F32)
    rest = x - hi
    mid = rest.astype(BF16).astype(F32)
    return hi, mid, (rest - mid).astype(BF16).astype(F32)


def _fox_operand(vals, e, cum, is_query):
    lane = lax.broadcasted_iota(jnp.int32, (1, LANES), 1)
    base = (1 - e) * FHD
    parts = _split3(cum)
    own = jnp.where(_head_mask(e), vals * FOX_SCALE if is_query else vals, 0.0)
    cum_at, ones_at = (base, base + 3) if is_query else (base + 3, base)
    sign = 1.0 if is_query else -1.0
    out = own + jnp.where((lane >= ones_at) & (lane < ones_at + 3), 1.0, 0.0)
    for i, part in enumerate(parts):
        out = out + jnp.where(lane == cum_at + i, sign * part, 0.0)
    return out.astype(BF16)


def _causal_block():
    return lax.broadcasted_iota(jnp.int32, (BQ, BQ), 1) <= lax.broadcasted_iota(jnp.int32, (BQ, BQ), 0)


def _head_rms(o, masks):
    o2 = o * o
    r = [lax.rsqrt(jnp.sum(jnp.where(mk, o2, 0.0), axis=1, keepdims=True) * (1.0 / FHD) + EPS) for mk in masks]
    return jnp.where(masks[0], r[0], r[1])


def _hosted(exchange):
    if exchange is None:
        return [], [], [], [], []
    return (exchange.inputs, [HBM] * len(exchange.inputs), [HBM] * len(exchange.out_shape), exchange.out_shape,
            exchange.sem_shapes())


def _fox_fwd(proj, gates, w2, exchange=None):
    ex_in, ex_in_specs, ex_out_specs, ex_out_shape, ex_scratch = _hosted(exchange)

    n_in = 3 * FOX_PAIRS + 2
    heads = [(pp, e) for pp in range(FOX_PAIRS) for e in range(2)]

    def body(*refs):
        qkv_refs, g_ref, w_ref = refs[:3 * FOX_PAIRS], refs[3 * FOX_PAIRS], refs[3 * FOX_PAIRS + 1]
        mix_ref, o_ref, lse_ref = refs[n_in + len(ex_in):n_in + 3 + len(ex_in)]
        ka_ref, vb_ref = refs[n_in + 3 + len(ex_in) + len(ex_out_shape):n_in + 5 + len(ex_in) + len(ex_out_shape)]
        ex_refs = refs[n_in:n_in + len(ex_in)] + refs[n_in + 3 + len(ex_in):n_in + 3 + len(ex_in) + len(ex_out_shape)] + refs[-2:]
        grp, qi = pl.program_id(0), pl.program_id(1)

        def head_index(pp, e):
            return 2 * (FOX_PAIRS * grp + pp) + e

        if exchange is not None:
            @pl.when((grp == 0) & (qi == 0))
            def _():
                exchange.start(*exchange.split(ex_refs))

        @pl.when(qi == 0)
        def _():
            gt = g_ref[...]
            for pp in range(FOX_PAIRS):
                kv = qkv_refs[3 * pp + 1][...]
                for e in range(2):
                    ka_ref[2 * pp + e] = _fox_operand(kv, e, _lane_col(gt, head_index(pp, e)), False)
                vb_ref[pp] = qkv_refs[3 * pp + 2][...].astype(BF16)

        masks = [_head_mask(0), _head_mask(1)]
        gt = g_ref[pl.ds(pl.multiple_of(qi * BQ, BQ), BQ), :]
        qs = [_fox_operand(qkv_refs[3 * pp][...], e, _lane_col(gt, head_index(pp, e)), True) for pp, e in heads]
        n = range(len(heads))

        def block(kj, carry, diagonal):
            rows = pl.ds(pl.multiple_of(kj * BQ, BQ), BQ)
            s = [_dot(qs[i], ka_ref[i, rows, :], 1, 1) for i in n]
            if diagonal:
                s = [jnp.where(_causal_block(), s[i], -jnp.inf) for i in n]
            m_new = [jnp.maximum(carry[i][0], jnp.max(s[i], axis=-1, keepdims=True)) for i in n]
            p = [jnp.exp(s[i] - m_new[i]) for i in n]
            alpha = [jnp.exp(carry[i][0] - m_new[i]) for i in n]
            l_new = [alpha[i] * carry[i][1] + jnp.sum(p[i], axis=-1, keepdims=True) for i in n]
            pv = [_dot(p[i], vb_ref[heads[i][0], rows, :]) for i in n]
            return tuple((m_new[i], l_new[i], alpha[i] * carry[i][2] + pv[i]) for i in n)

        one = (jnp.full((BQ, 1), -jnp.inf, F32), jnp.zeros((BQ, 1), F32), jnp.zeros((BQ, LANES), F32))
        below = lax.fori_loop(0, qi, lambda kj, carry: block(kj, carry, False), (one,) * len(heads))
        done = block(qi, below, True)
        for pp in range(FOX_PAIRS):
            (m0, l0, a0), (m1, l1, a1) = done[2 * pp], done[2 * pp + 1]
            o = jnp.where(masks[0], a0 / l0, a1 / l1)
            cols = slice(pp * LANES, (pp + 1) * LANES)
            o_ref[:, cols] = o
            mix_ref[:, cols] = (o * _head_rms(o, masks) * w_ref[...]).astype(BF16)
            lse_ref[pp] = _pair_cols(m0 + jnp.log(l0), m1 + jnp.log(l1))

        if exchange is not None:
            @pl.when((grp == NPAIR // FOX_PAIRS // 2) & (qi == 0))
            def _():
                exchange.middle(*exchange.split(ex_refs))

            @pl.when((grp == NPAIR // FOX_PAIRS - 1) & (qi == NQ - 1))
            def _():
                exchange.rest(*exchange.split(ex_refs))

    qkv_specs = []
    for pp in range(FOX_PAIRS):
        qkv_specs.append(pl.BlockSpec((BQ, LANES), lambda g, i, pp=pp: (i, 3 * (FOX_PAIRS * g + pp))))
        qkv_specs.append(pl.BlockSpec((S, LANES), lambda g, i, pp=pp: (0, 3 * (FOX_PAIRS * g + pp) + 1)))
        qkv_specs.append(pl.BlockSpec((S, LANES), lambda g, i, pp=pp: (0, 3 * (FOX_PAIRS * g + pp) + 2)))
    blk = pl.BlockSpec((BQ, FOX_PAIRS * LANES), lambda g, i: (i, g))
    res = pl.pallas_call(
        body, name="fox_fwd", grid=(NPAIR // FOX_PAIRS, NQ),
        in_specs=qkv_specs + [pl.BlockSpec((S, LANES), lambda g, i: (0, 0)), pl.BlockSpec((1, LANES), lambda g, i: (0, 0))]
        + ex_in_specs,
        out_specs=[blk, blk, pl.BlockSpec((FOX_PAIRS, BQ, 2), lambda g, i: (g, i, 0))] + ex_out_specs,
        out_shape=[jax.ShapeDtypeStruct((S, D), BF16), jax.ShapeDtypeStruct((S, DFOX), F32),
                   jax.ShapeDtypeStruct((NPAIR, S, 2), F32)] + ex_out_shape,
        scratch_shapes=[pltpu.VMEM((2 * FOX_PAIRS, S, LANES), BF16), pltpu.VMEM((FOX_PAIRS, S, LANES), BF16)] + ex_scratch,
        compiler_params=_cparams(),
    )(*([proj] * (3 * FOX_PAIRS)), gates, w2, *ex_in)
    return res[0], res[1], res[2], res[3:]


def _fox_norm_bwd(o, dmix, w2, exchange=None):
    ex_in, ex_in_specs, ex_out_specs, ex_out_shape, ex_scratch = _hosted(exchange)

    def body(*refs):
        o_ref, g_ref, w_ref = refs[:3]
        do_ref, dl_ref, dw_ref = refs[3 + len(ex_in):6 + len(ex_in)]
        ex_refs = refs[3:3 + len(ex_in)] + refs[6 + len(ex_in):]
        hp, qi = pl.program_id(0), pl.program_id(1)

        if exchange is not None:
            @pl.when((hp == 0) & (qi == 0))
            def _():
                exchange.start(*exchange.split(ex_refs))

        masks = [_head_mask(0), _head_mask(1)]
        ov = o_ref[...]
        g = g_ref[...]
        r = _head_rms(ov, masks)
        gw = g * w_ref[...]
        gwo = gw * ov
        mean = [jnp.sum(jnp.where(mk, gwo, 0.0), axis=1, keepdims=True) * (1.0 / FHD) for mk in masks]
        do = r * gw - ov * (r * r * r) * jnp.where(masks[0], mean[0], mean[1])
        do_ref[...] = do.astype(BF16)
        doo = do * ov
        dl_ref[...] = _pair_cols(*[jnp.sum(jnp.where(mk, doo, 0.0), axis=1, keepdims=True) for mk in masks])

        @pl.when((hp == 0) & (qi == 0))
        def _():
            dw_ref[...] = jnp.zeros_like(dw_ref)

        dw_ref[...] += jnp.sum(g * ov * r, axis=0, keepdims=True)

        @pl.when((hp == NPAIR - 1) & (qi == NQ - 1))
        def _():
            dw = dw_ref[...]
            dw_ref[...] = dw + pltpu.roll(dw, FHD, 1)
            if exchange is not None:
                exchange.finish(*exchange.split(ex_refs))

    blk = pl.BlockSpec((BQ, LANES), lambda hp, i: (i, hp))
    vec = pl.BlockSpec((1, LANES), lambda hp, i: (0, 0))
    res = pl.pallas_call(
        body, name="fox_norm_bwd", grid=(NPAIR, NQ), in_specs=[blk, blk, vec] + ex_in_specs,
        out_specs=[blk, pl.BlockSpec((None, BQ, 2), lambda hp, i: (hp, i, 0)), vec] + ex_out_specs,
        out_shape=[jax.ShapeDtypeStruct((S, DFOX), BF16), jax.ShapeDtypeStruct((NPAIR, S, 2), F32),
                   jax.ShapeDtypeStruct((1, LANES), F32)] + ex_out_shape,
        scratch_shapes=ex_scratch, compiler_params=_cparams(),
    )(o, dmix, w2, *ex_in)
    return res[0], res[1], res[2], res[3:]


def _fox_bwd(proj, do, gates, lse, delta, exchange=None):
    ex_in, ex_in_specs, ex_out_specs, ex_out_shape, ex_scratch = _hosted(exchange)

    pg = FOX_PAIRS_BWD
    n_in = 3 * pg + 4
    heads = [(pp, e) for pp in range(pg) for e in range(2)]

    def body(*refs):
        qkv_refs = refs[:3 * pg]
        do_ref, g_ref, lse_ref, dl_ref = refs[3 * pg:n_in]
        dproj_ref, dc_ref = refs[n_in + len(ex_in):n_in + 2 + len(ex_in)]
        qa_ref, dq_ref = refs[n_in + 2 + len(ex_in) + len(ex_out_shape):n_in + 4 + len(ex_in) + len(ex_out_shape)]
        ex_refs = refs[n_in:n_in + len(ex_in)] + refs[n_in + 2 + len(ex_in):n_in + 2 + len(ex_in) + len(ex_out_shape)] + refs[-2:]
        grp, kj = pl.program_id(0), pl.program_id(1)

        def head_index(pp, e):
            return 2 * (pg * grp + pp) + e

        if exchange is not None:
            @pl.when((grp == 0) & (kj == 0))
            def _():
                exchange.start(*exchange.split(ex_refs))

        @pl.when(kj == 0)
        def _():
            gt = g_ref[...]
            for pp in range(pg):
                qv = qkv_refs[3 * pp][...]
                for e in range(2):
                    qa_ref[2 * pp + e] = _fox_operand(qv, e, _lane_col(gt, head_index(pp, e)), True)
            dq_ref[...] = jnp.zeros_like(dq_ref)

        @pl.when((grp == 0) & (kj == 0))
        def _():
            dc_ref[...] = jnp.zeros_like(dc_ref)

        masks = [_head_mask(0), _head_mask(1)]
        krows = pl.ds(pl.multiple_of(kj * BQ, BQ), BQ)
        gk = g_ref[krows, :]
        kas = [_fox_operand(qkv_refs[3 * pp + 1][...], e, _lane_col(gk, head_index(pp, e)), False) for pp, e in heads]
        vbs = [qkv_refs[3 * pp + 2][...].astype(BF16) for pp in range(pg)]
        lane = lax.broadcasted_iota(jnp.int32, (BQ, LANES), 1)
        n = range(len(heads))

        def block(qi, carry, diagonal):
            dks, dvs, css = carry
            rows = pl.ds(pl.multiple_of(qi * BQ, BQ), BQ)
            qa = [qa_ref[i, rows, :] for i in n]
            s = [_dot(qa[i], kas[i], 1, 1) for i in n]
            if diagonal:
                s = [jnp.where(_causal_block(), s[i], -jnp.inf) for i in n]
            dov = [do_ref[rows, pp * LANES:(pp + 1) * LANES] for pp in range(pg)]
            doe = [jnp.where(masks[e], dov[pp], jnp.zeros_like(dov[pp])) for pp, e in heads]
            lse2 = [lse_ref[pp, rows, :] for pp in range(pg)]
            dl2 = [dl_ref[pp, rows, :] for pp in range(pg)]
            p = [jnp.exp(s[i] - _lane_col(lse2[heads[i][0]], heads[i][1])) for i in n]
            dp = [_dot(doe[i], vbs[heads[i][0]], 1, 1) for i in n]
            ds = [p[i] * (dp[i] - _lane_col(dl2[heads[i][0]], heads[i][1])) for i in n]
            dv_part = [_dot(p[i], doe[i], 0, 0) for i in n]
            dk_part = [_dot(ds[i], jnp.where(masks[heads[i][1]], qa[i], jnp.zeros_like(qa[i])), 0, 0) for i in n]
            dq_part = [jnp.where(masks[heads[i][1]], _dot(ds[i], kas[i]), 0.0) for i in n]
            css = tuple(css[i] + jnp.sum(ds[i], axis=0, keepdims=True) for i in n)
            dc = jnp.zeros((BQ, LANES), F32)
            for i in n:
                dc = dc + jnp.where(lane == head_index(*heads[i]), jnp.sum(ds[i], axis=1, keepdims=True), 0.0)
            for pp in range(pg):
                dq_ref[pp, rows, :] += (dq_part[2 * pp] + dq_part[2 * pp + 1]) * FOX_SCALE
            dc_ref[rows, :] += dc
            dks = tuple(dks[pp] + dk_part[2 * pp] + dk_part[2 * pp + 1] for pp in range(pg))
            dvs = tuple(dvs[pp] + dv_part[2 * pp] + dv_part[2 * pp + 1] for pp in range(pg))
            return dks, dvs, css

        zero = jnp.zeros((BQ, LANES), F32)
        first = block(kj, ((zero,) * pg, (zero,) * pg, (jnp.zeros((1, BQ), F32),) * len(heads)), True)
        dks, dvs, css = lax.fori_loop(kj + 1, NQ, lambda qi, carry: block(qi, carry, False), first)
        r = lax.broadcasted_iota(jnp.int32, (BQ, BQ), 0)
        c = lax.broadcasted_iota(jnp.int32, (BQ, BQ), 1)
        dcol = jnp.zeros((BQ, LANES), F32)
        for i in n:
            col = jnp.sum(jnp.where(r == c, css[i], 0.0), axis=1, keepdims=True)
            dcol = dcol + jnp.where(lane == head_index(*heads[i]), col, 0.0)
        dc_ref[krows, :] -= dcol
        for pp in range(pg):
            base = 3 * pp * LANES
            dproj_ref[krows, base + LANES:base + 2 * LANES] = dks[pp].astype(BF16)
            dproj_ref[krows, base + 2 * LANES:base + 3 * LANES] = dvs[pp].astype(BF16)

        @pl.when(kj == NQ - 1)
        def _():
            for pp in range(pg):
                dproj_ref[:, 3 * pp * LANES:(3 * pp + 1) * LANES] = dq_ref[pp].astype(BF16)

        if exchange is not None:
            @pl.when((grp == NPAIR // pg // 2) & (kj == 0))
            def _():
                exchange.middle(*exchange.split(ex_refs))

            @pl.when((grp == NPAIR // pg - 1) & (kj == NQ - 1))
            def _():
                exchange.rest(*exchange.split(ex_refs))

    qkv_specs = []
    for pp in range(pg):
        qkv_specs.append(pl.BlockSpec((S, LANES), lambda g, j, pp=pp: (0, 3 * (pg * g + pp))))
        qkv_specs.append(pl.BlockSpec((BQ, LANES), lambda g, j, pp=pp: (j, 3 * (pg * g + pp) + 1)))
        qkv_specs.append(pl.BlockSpec((BQ, LANES), lambda g, j, pp=pp: (j, 3 * (pg * g + pp) + 2)))
    pair = pl.BlockSpec((pg, S, 2), lambda g, j: (g, 0, 0))
    res = pl.pallas_call(
        body, name="fox_bwd", grid=(NPAIR // pg, NQ),
        in_specs=qkv_specs + [pl.BlockSpec((S, pg * LANES), lambda g, j: (0, g)), pl.BlockSpec((S, LANES), lambda g, j: (0, 0)),
                              pair, pair] + ex_in_specs,
        out_specs=[pl.BlockSpec((S, 3 * pg * LANES), lambda g, j: (0, g)), pl.BlockSpec((S, LANES), lambda g, j: (0, 0))]
        + ex_out_specs,
        out_shape=[jax.ShapeDtypeStruct((S, DPROJ_PAD), BF16), jax.ShapeDtypeStruct((S, LANES), F32)] + ex_out_shape,
        scratch_shapes=[pltpu.VMEM((2 * pg, S, LANES), BF16), pltpu.VMEM((pg, S, LANES), F32)] + ex_scratch,
        compiler_params=_cparams(),
    )(*([proj] * (3 * pg)), do, gates, lse, delta, *ex_in)
    return res[0], res[1], res[2:]


NQKV = 3 * NGH
GDN_QSCALE = GHD ** -0.5


def _shift_down(x, s):
    if s == 0:
        return x
    row = lax.broadcasted_iota(jnp.int32, x.shape, 0)
    return jnp.where(row >= s, pltpu.roll(x, s, 0), 0.0)


def _shift_up(x, s):
    if s == 0:
        return x
    n = x.shape[0]
    row = lax.broadcasted_iota(jnp.int32, x.shape, 0)
    return jnp.where(row < n - s, pltpu.roll(x, n - s, 0), 0.0)


def _conv_taps(xv):
    return [_shift_down(xv, CONV_K - 1 - j) for j in range(CONV_K)]


def _conv_pre(taps, wv):
    pre = taps[CONV_K - 1] * wv[CONV_K - 1:CONV_K, :]
    for j in range(CONV_K - 1):
        pre = pre + taps[j] * wv[j:j + 1, :]
    return pre


def _l2_factors(b):
    return b < 2 * NGH, jnp.where(b < NGH, GDN_QSCALE, 1.0)


def _gdn_pre(proj, conv_w):
    def body(x_ref, w_ref, o_ref):
        b = pl.program_id(0)
        c = _silu(_conv_pre(_conv_taps(x_ref[...]), w_ref[...]))
        normed, scale = _l2_factors(b)
        rs = lax.rsqrt(jnp.sum(c * c, axis=-1, keepdims=True) + EPS)
        o_ref[...] = c * jnp.where(normed, rs, 1.0) * scale

    return pl.pallas_call(
        body, name="gdn_pre", grid=(NQKV,),
        in_specs=[pl.BlockSpec((S, GHD), lambda b: (0, BLK_GDN + b)), pl.BlockSpec((CONV_K, GHD), lambda b: (0, b))],
        out_specs=pl.BlockSpec((S, GHD), lambda b: (0, b)),
        out_shape=jax.ShapeDtypeStruct((S, NQKV * GHD), F32), compiler_params=_cparams(),
    )(proj, conv_w)


def _gdn_pre_bwd(proj, conv_w, dqkv, dproj):
    def body(x_ref, w_ref, dy_ref, dproj_in, dx_ref, dw_ref):
        del dproj_in
        b = pl.program_id(0)
        taps = _conv_taps(x_ref[...])
        wv = w_ref[...]
        pre = _conv_pre(taps, wv)
        sig = _sigmoid(pre)
        c = pre * sig
        normed, scale = _l2_factors(b)
        g = dy_ref[...] * scale
        rs = lax.rsqrt(jnp.sum(c * c, axis=-1, keepdims=True) + EPS)
        dc_n = rs * g - c * (rs * rs * rs) * jnp.sum(g * c, axis=-1, keepdims=True)
        dc = jnp.where(normed, dc_n, g)
        dpre = dc * sig * (1.0 + pre * (1.0 - sig))
        dx = dpre * wv[CONV_K - 1:CONV_K, :]
        for j in range(CONV_K - 1):
            dx = dx + _shift_up(dpre, CONV_K - 1 - j) * wv[j:j + 1, :]
        dx_ref[...] = dx.astype(BF16)
        for j in range(CONV_K):
            dw_ref[j:j + 1, :] = jnp.sum(dpre * taps[j], axis=0, keepdims=True)

    return pl.pallas_call(
        body, name="gdn_pre_bwd", grid=(NQKV,),
        in_specs=[pl.BlockSpec((S, GHD), lambda b: (0, BLK_GDN + b)), pl.BlockSpec((CONV_K, GHD), lambda b: (0, b)),
                  pl.BlockSpec((None, S, GHD), lambda b: (b // NGH, 0, b % NGH)), pl.BlockSpec(memory_space=pl.ANY)],
        out_specs=[pl.BlockSpec((S, GHD), lambda b: (0, BLK_GDN + b)), pl.BlockSpec((CONV_K, GHD), lambda b: (0, b))],
        out_shape=[jax.ShapeDtypeStruct((S, DPROJ_PAD), BF16), jax.ShapeDtypeStruct((CONV_K, NQKV * GHD), F32)],
        input_output_aliases={3: 0}, compiler_params=_cparams(),
    )(proj, conv_w, dqkv, dproj)


CB = 16
NCB = NCH // CB


def _chunk_prep(qs, ks, vs, gcols, bcols, t_saved=None):
    n = range(len(qs))
    r = lax.broadcasted_iota(jnp.int32, (CHUNK, CHUNK), 0)
    c = lax.broadcasted_iota(jnp.int32, (CHUNK, CHUNK), 1)
    incl = c <= r
    eye = (r == c).astype(F32)
    grow = [jnp.sum(gcols[i] * eye, axis=0, keepdims=True) for i in n]
    gc_col = [jnp.sum(jnp.where(incl, grow[i], 0.0), axis=1, keepdims=True) for i in n]
    gc_row = [jnp.sum(jnp.where(r <= c, gcols[i], 0.0), axis=0, keepdims=True) for i in n]
    decay = [jnp.exp(jnp.where(incl, gc_col[i] - gc_row[i], -jnp.inf)) for i in n]
    kb = [ks[i] * bcols[i] for i in n]
    vb = [vs[i] * bcols[i] for i in n]
    kk = [_mm_nt(kb[i], ks[i]) for i in n]
    m = [jnp.where(c < r, kk[i] * decay[i], 0.0) for i in n]
    if t_saved is None:
        t_inv = [eye - m[i] for i in n]
        p = [_dot3(m[i], m[i]) for i in n]
        for step in range(5):
            t_inv = [t_inv[i] + _dot3(t_inv[i], p[i]) for i in n]
            if step < 4:
                p = [_dot3(p[i], p[i]) for i in n]
    else:
        t_inv = [_saved_inverse(m[i], t_saved[i]) for i in n]
    egc = [jnp.exp(gc_col[i]) for i in n]
    u = [_mm_nn(t_inv[i], vb[i]) for i in n]
    w = [_mm_nn(t_inv[i], kb[i] * egc[i]) for i in n]
    qk = [_mm_nt(qs[i], ks[i]) for i in n]
    gc_last = [gc_col[i][CHUNK - 1:CHUNK, :] for i in n]
    return [(u[i], w[i], qk[i] * decay[i], qs[i] * egc[i], ks[i] * jnp.exp(gc_last[i] - gc_col[i]), jnp.exp(gc_last[i]),
             t_inv[i]) for i in n]


def _prep_specs():
    rows = CB * CHUNK
    qs = pl.BlockSpec((rows, GHD), lambda i, h: (i, h))
    ks = pl.BlockSpec((rows, GHD), lambda i, h: (i, NGH + h))
    vs = pl.BlockSpec((rows, GHD), lambda i, h: (i, 2 * NGH + h))
    gs = pl.BlockSpec((rows, LANES), lambda i, h: (i, 0))
    a_s = pl.BlockSpec((None, rows, CHUNK), lambda i, h: (h, i, 0))
    gl_s = pl.BlockSpec((None, CB, 1, LANES), lambda i, h: (h, i, 0, 0))
    return qs, ks, vs, gs, a_s, gl_s


def _gdn_prep(qkv, gates, exchange=None):
    ex_in, ex_in_specs, ex_out_specs, ex_out_shape, ex_scratch = _hosted(exchange)

    def body(*refs):
        q_ref, k_ref, v_ref, g_ref = refs[:4]
        u_ref, w_ref, qd_ref, kd_ref, a_ref, gl_ref, t_ref = refs[4 + len(ex_in):11 + len(ex_in)]
        ex_refs = refs[4:4 + len(ex_in)] + refs[11 + len(ex_in):]
        h = pl.program_id(1)

        if exchange is not None:
            @pl.when((pl.program_id(0) == 0) & (h == 0))
            def _():
                exchange.start(*exchange.split(ex_refs))

        chunks = [pl.ds(cidx * CHUNK, CHUNK) for cidx in range(CB)]
        gts = [g_ref[rows, :] for rows in chunks]
        outs = _chunk_prep([q_ref[rows, :] for rows in chunks], [k_ref[rows, :] for rows in chunks],
                           [v_ref[rows, :] for rows in chunks], [_lane_col(gt, LANE_G + h) for gt in gts],
                           [_lane_col(gt, LANE_BETA + h) for gt in gts])
        for cidx, rows in enumerate(chunks):
            u, w, a, qd, kd, gl, t_inv = outs[cidx]
            u_ref[rows, :] = u
            w_ref[rows, :] = w
            qd_ref[rows, :] = qd
            kd_ref[rows, :] = kd
            a_ref[rows, :] = a
            t_ref[rows, :] = t_inv
            gl_ref[cidx] = jnp.broadcast_to(gl, (1, LANES))

        if exchange is not None:
            @pl.when((pl.program_id(0) == NCB // 2) & (h == 0))
            def _():
                exchange.middle(*exchange.split(ex_refs))

            @pl.when((pl.program_id(0) == NCB - 1) & (h == NGH - 1))
            def _():
                exchange.rest(*exchange.split(ex_refs))

    qs, ks, vs, gs, a_s, gl_s = _prep_specs()
    tok = jax.ShapeDtypeStruct((S, DGDN), F32)
    sq = jax.ShapeDtypeStruct((NGH, S, CHUNK), F32)
    res = pl.pallas_call(
        body, name="gdn_prep", grid=(NCB, NGH), in_specs=[qs, ks, vs, gs] + ex_in_specs,
        out_specs=[qs, qs, qs, qs, a_s, gl_s, a_s] + ex_out_specs,
        out_shape=[tok, tok, tok, tok, sq, jax.ShapeDtypeStruct((NGH, NCH, 1, LANES), F32), sq] + ex_out_shape,
        scratch_shapes=ex_scratch, compiler_params=_cparams(),
    )(qkv, qkv, qkv, gates, *ex_in)
    return res[:7], res[7:]


def _gdn_prep_bwd(qkv, gates, t_inv, du, dw, dqd, dkd, da, dgl, exchange=None):
    ex_in, ex_in_specs, ex_out_specs, ex_out_shape, ex_scratch = _hosted(exchange)

    def body(*refs):
        q_ref, k_ref, v_ref, g_ref, t_ref, du_ref, dw_ref, dqd_ref, dkd_ref, da_ref, dgl_ref = refs[:11]
        dqkv_ref, dg_ref = refs[11 + len(ex_in):13 + len(ex_in)]
        ex_refs = refs[11:11 + len(ex_in)] + refs[13 + len(ex_in):]
        h = pl.program_id(1)

        if exchange is not None:
            @pl.when((pl.program_id(0) == 0) & (h == 0))
            def _():
                exchange.start(*exchange.split(ex_refs))

        @pl.when(h == 0)
        def _():
            dg_ref[...] = jnp.zeros_like(dg_ref)

        lane = lax.broadcasted_iota(jnp.int32, (CHUNK, LANES), 1)
        chunks = [pl.ds(cidx * CHUNK, CHUNK) for cidx in range(CB)]
        gts = [g_ref[rows, :] for rows in chunks]
        t_saved = [t_ref[rows, :] for rows in chunks]
        _, vjp = jax.vjp(lambda *args: [o[:6] for o in _chunk_prep(*args, t_saved=t_saved)],
                         [q_ref[rows, :] for rows in chunks], [k_ref[rows, :] for rows in chunks],
                         [v_ref[rows, :] for rows in chunks], [_lane_col(gt, LANE_G + h) for gt in gts],
                         [_lane_col(gt, LANE_BETA + h) for gt in gts])
        dqs, dks, dvs, dgcs, dbcs = vjp([(du_ref[rows, :], dw_ref[rows, :], da_ref[rows, :], dqd_ref[rows, :],
                                          dkd_ref[rows, :], dgl_ref[cidx][:, 0:1]) for cidx, rows in enumerate(chunks)])
        for cidx, rows in enumerate(chunks):
            dq, dk, dv, dgc, dbc = dqs[cidx], dks[cidx], dvs[cidx], dgcs[cidx], dbcs[cidx]
            dqkv_ref[0, rows, :] = dq
            dqkv_ref[1, rows, :] = dk
            dqkv_ref[2, rows, :] = dv
            dg_ref[rows, :] += jnp.where(lane == LANE_G + h, dgc, 0.0) + jnp.where(lane == LANE_BETA + h, dbc, 0.0)

        if exchange is not None:
            @pl.when((pl.program_id(0) == NCB - 1) & (h == NGH - 1))
            def _():
                exchange.finish(*exchange.split(ex_refs))

    qs, ks, vs, gs, a_s, gl_s = _prep_specs()
    res = pl.pallas_call(
        body, name="gdn_prep_bwd", grid=(NCB, NGH), in_specs=[qs, ks, vs, gs, a_s, qs, qs, qs, qs, a_s, gl_s] + ex_in_specs,
        out_specs=[pl.BlockSpec((3, CB * CHUNK, GHD), lambda i, h: (0, i, h)), gs] + ex_out_specs,
        out_shape=[jax.ShapeDtypeStruct((3, S, DGDN), F32), jax.ShapeDtypeStruct((S, LANES), F32)] + ex_out_shape,
        scratch_shapes=ex_scratch, compiler_params=_cparams(),
    )(qkv, qkv, qkv, gates, t_inv, du, dw, dqd, dkd, da, dgl, *ex_in)
    return res[0], res[1], res[2:]


def _scan_specs(nh, parts, reverse):
    wide, rows, chunks = nh * GHD, S // parts, NCH // parts

    def part(p):
        return parts - 1 - p if reverse else p

    hs = pl.BlockSpec((rows, wide), lambda g, p: (part(p), g))
    a_s = pl.BlockSpec((nh, rows, CHUNK), lambda g, p: (g, part(p), 0))
    gl_s = pl.BlockSpec((nh, chunks, 1, LANES), lambda g, p: (g, part(p), 0, 0))
    st_s = pl.BlockSpec((nh, chunks, GHD, GHD), lambda g, p: (g, part(p), 0, 0))
    gz_s = pl.BlockSpec((rows, wide), lambda g, p: (part(p), BLK_GZ // nh + g))
    mix_s = pl.BlockSpec((rows, wide), lambda g, p: (part(p), NPAIR // nh + g))
    return hs, a_s, gl_s, st_s, gz_s, mix_s


def _head_cols(hh):
    return slice(hh * GHD, (hh + 1) * GHD)


SCAN_HEADS, SCAN_PARTS = 4, 2
SCAN_HEADS_BWD, SCAN_PARTS_BWD = 4, 4


def _gdn_scan(u, w, qd, kd, a, gl, proj, w_norm, mix):
    heads = range(SCAN_HEADS)

    def body(u_ref, w_ref, qd_ref, kd_ref, a_ref, gl_ref, z_ref, wn_ref, mix_in, mix_ref, o_ref, st_ref, carry_ref):
        del mix_in

        @pl.when(pl.program_id(1) == 0)
        def _():
            carry_ref[...] = jnp.zeros_like(carry_ref)

        def step(ci, states):
            rows = pl.ds(pl.multiple_of(ci * CHUNK, CHUNK), CHUNK)
            for hh in heads:
                st_ref[hh, ci] = states[hh]
            ws = [_dot(w_ref[rows, _head_cols(hh)], states[hh]) for hh in heads]
            qs = [_dot(qd_ref[rows, _head_cols(hh)], states[hh]) for hh in heads]
            vn = [u_ref[rows, _head_cols(hh)] - ws[hh] for hh in heads]
            av = [_dot(a_ref[hh, rows, :], vn[hh]) for hh in heads]
            kv = [_dot(kd_ref[rows, _head_cols(hh)], vn[hh], 0, 0) for hh in heads]
            for hh in heads:
                o_ref[rows, _head_cols(hh)] = qs[hh] + av[hh]
            return tuple(states[hh] * gl_ref[hh, ci] + kv[hh] for hh in heads)

        last = lax.fori_loop(0, NCH // SCAN_PARTS, step, tuple(carry_ref[hh] for hh in heads))
        for hh in heads:
            carry_ref[hh] = last[hh]
            ov = o_ref[:, _head_cols(hh)]
            mix_ref[:, _head_cols(hh)] = (ov * _rms_scale(ov) * wn_ref[...] * _silu(z_ref[:, _head_cols(hh)])).astype(BF16)

    hs, a_s, gl_s, st_s, gz_s, mix_s = _scan_specs(SCAN_HEADS, SCAN_PARTS, False)
    return pl.pallas_call(
        body, name="gdn_scan", grid=(NGH // SCAN_HEADS, SCAN_PARTS),
        in_specs=[hs, hs, hs, hs, a_s, gl_s, gz_s, pl.BlockSpec((1, GHD), lambda g, p: (0, 0)),
                  pl.BlockSpec(memory_space=pl.ANY)],
        out_specs=[mix_s, hs, st_s],
        out_shape=[jax.ShapeDtypeStruct((S, D), BF16), jax.ShapeDtypeStruct((S, DGDN), F32),
                   jax.ShapeDtypeStruct((NGH, NCH, GHD, GHD), F32)],
        input_output_aliases={8: 0}, scratch_shapes=[pltpu.VMEM((SCAN_HEADS, GHD, GHD), F32)], compiler_params=_cparams(),
    )(u, w, qd, kd, a, gl, proj, w_norm, mix)


def _gdn_scan_bwd(dmix, o, proj, w_norm, u, w, qd, kd, a, gl, states, dproj, exchange=None):
    ex_in, ex_in_specs, ex_out_specs, ex_out_shape, ex_scratch = _hosted(exchange)
    groups = NGH // SCAN_HEADS_BWD

    def body(*refs):
        dy_ref, o_ref, z_ref, wn_ref, u_ref, w_ref, qd_ref, kd_ref, a_ref, gl_ref, st_ref = refs[:11]
        dz_ref, du_ref, dw_ref, dqd_ref, dkd_ref, da_ref, dgl_ref, dwn_ref = refs[12 + len(ex_in):20 + len(ex_in)]
        do_ref, carry_ref = refs[20 + len(ex_in) + len(ex_out_shape):22 + len(ex_in) + len(ex_out_shape)]
        ex_refs = refs[12:12 + len(ex_in)] + refs[20 + len(ex_in):20 + len(ex_in) + len(ex_out_shape)] + refs[-2:]
        heads = range(SCAN_HEADS_BWD)
        chunks = NCH // SCAN_PARTS_BWD

        if exchange is not None:
            @pl.when((pl.program_id(0) == 0) & (pl.program_id(1) == 0))
            def _():
                exchange.start(*exchange.split(ex_refs))

        @pl.when((pl.program_id(0) == 0) & (pl.program_id(1) == 0))
        def _():
            dwn_ref[...] = jnp.zeros_like(dwn_ref)

        @pl.when(pl.program_id(1) == 0)
        def _():
            carry_ref[...] = jnp.zeros_like(carry_ref)

        wn = wn_ref[...]
        for hh in heads:
            c = _head_cols(hh)
            ov = o_ref[:, c]
            zv = z_ref[:, c]
            g = dy_ref[:, c]
            sig = _sigmoid(zv)
            dz_ref[:, c] = (g * (ov * _rms_scale(ov) * wn) * sig * (1.0 + zv * (1.0 - sig))).astype(BF16)
            do, dwt = _rms_bwd(ov, wn, g * zv * sig)
            do_ref[:, c] = do
            dwn_ref[...] += jnp.sum(dwt, axis=0, keepdims=True)

        def step(t, dstates):
            ci = chunks - 1 - t
            rows = pl.ds(pl.multiple_of(ci * CHUNK, CHUNK), CHUNK)
            cols = [_head_cols(hh) for hh in heads]
            state = [st_ref[hh, ci] for hh in heads]
            dov = [do_ref[rows, cols[hh]] for hh in heads]
            wv = [w_ref[rows, cols[hh]] for hh in heads]
            ws = [_dot(wv[hh], state[hh]) for hh in heads]
            adov = [_dot(a_ref[hh, rows, :], dov[hh], 0, 0) for hh in heads]
            kds = [_dot(kd_ref[rows, cols[hh]], dstates[hh]) for hh in heads]
            dqd = [_dot(dov[hh], state[hh], 1, 1) for hh in heads]
            qdo = [_dot(qd_ref[rows, cols[hh]], dov[hh], 0, 0) for hh in heads]
            vn = [u_ref[rows, cols[hh]] - ws[hh] for hh in heads]
            dvn = [adov[hh] + kds[hh] for hh in heads]
            da = [_dot(dov[hh], vn[hh], 1, 1) for hh in heads]
            dkd = [_dot(vn[hh], dstates[hh], 1, 1) for hh in heads]
            dwv = [_dot(dvn[hh], state[hh], 1, 1) for hh in heads]
            wdv = [_dot(wv[hh], dvn[hh], 0, 0) for hh in heads]
            for hh in heads:
                da_ref[hh, rows, :] = da[hh]
                dqd_ref[rows, cols[hh]] = dqd[hh]
                dkd_ref[rows, cols[hh]] = dkd[hh]
                dgl = jnp.sum(jnp.sum(dstates[hh] * state[hh], axis=1, keepdims=True), axis=0, keepdims=True)
                dgl_ref[hh, ci] = jnp.broadcast_to(dgl, (1, LANES))
                du_ref[rows, cols[hh]] = dvn[hh]
                dw_ref[rows, cols[hh]] = -dwv[hh]
            return tuple(dstates[hh] * gl_ref[hh, ci] + qdo[hh] - wdv[hh] for hh in heads)

        last = lax.fori_loop(0, chunks, step, tuple(carry_ref[hh] for hh in heads))
        for hh in heads:
            carry_ref[hh] = last[hh]

        if exchange is not None:
            @pl.when((pl.program_id(0) == groups - 1) & (pl.program_id(1) == SCAN_PARTS_BWD - 1))
            def _():
                exchange.finish(*exchange.split(ex_refs))

    hs, a_s, gl_s, st_s, gz_s, mix_s = _scan_specs(SCAN_HEADS_BWD, SCAN_PARTS_BWD, True)
    vec = pl.BlockSpec((1, GHD), lambda g, p: (0, 0))
    tok = jax.ShapeDtypeStruct((S, DGDN), F32)
    res = pl.pallas_call(
        body, name="gdn_scan_bwd", grid=(groups, SCAN_PARTS_BWD),
        in_specs=[mix_s, hs, gz_s, vec, hs, hs, hs, hs, a_s, gl_s, st_s, pl.BlockSpec(memory_space=pl.ANY)] + ex_in_specs,
        out_specs=[gz_s, hs, hs, hs, hs, a_s, gl_s, vec] + ex_out_specs,
        out_shape=[jax.ShapeDtypeStruct((S, DPROJ_PAD), BF16), tok, tok, tok, tok,
                   jax.ShapeDtypeStruct((NGH, S, CHUNK), F32), jax.ShapeDtypeStruct((NGH, NCH, 1, LANES), F32),
                   jax.ShapeDtypeStruct((1, GHD), F32)] + ex_out_shape,
        input_output_aliases={11: 0},
        scratch_shapes=[pltpu.VMEM((S // SCAN_PARTS_BWD, SCAN_HEADS_BWD * GHD), F32),
                        pltpu.VMEM((SCAN_HEADS_BWD, GHD, GHD), F32)] + ex_scratch,
        compiler_params=_cparams(),
    )(dmix, o, proj, w_norm, u, w, qd, kd, a, gl, states, dproj, *ex_in)
    return res[:8], res[8:]


def _place():
    return lax.axis_index("x"), lax.axis_index("y"), lax.axis_index("c")


def _other_chips(x, y):
    return [(1 - x, y), (x, 1 - y), (1 - x, 1 - y)]


HBM = pl.BlockSpec(memory_space=pltpu.HBM)
VMEM = pl.BlockSpec(memory_space=pltpu.VMEM)


def _half_rows(ref_or_rows, half):
    rows = ref_or_rows // 2
    return pl.ds(pl.multiple_of(half * rows, rows), rows)


class _Exchange:
    def __init__(self, inputs, out_shape, n_sems, start, finish=None, middle=None, rest=None):
        self.inputs, self.out_shape, self.n_sems, self.start = inputs, out_shape, n_sems, start
        if finish is None:
            def finish(*refs):
                middle(*refs)
                rest(*refs)
        self.finish = finish
        self.middle = middle if middle is not None else (lambda *refs: None)
        self.rest = rest if rest is not None else finish

    def sem_shapes(self):
        return [pltpu.SemaphoreType.DMA((self.n_sems,)), pltpu.SemaphoreType.DMA((self.n_sems,))]

    def split(self, refs):
        n_in, n_out = len(self.inputs), len(self.out_shape)
        return refs[:n_in], refs[n_in:n_in + n_out], refs[n_in + n_out], refs[n_in + n_out + 1]


def _run_exchange(ex, name):
    def body(*refs):
        parts = ex.split(refs)
        ex.start(*parts)
        ex.finish(*parts)

    return pl.pallas_call(
        body, name=name, in_specs=[HBM] * len(ex.inputs), out_specs=[HBM] * len(ex.out_shape), out_shape=ex.out_shape,
        scratch_shapes=ex.sem_shapes(), compiler_params=_cparams(),
    )(*ex.inputs)


def _allgather_exchange(shards, whole=()):
    n, nw = len(shards), len(whole)
    slots = 8

    def plan(src, outs, send_sems, recv_sems):
        x, y, c = _place()
        via_x, via_y, diagonal = _other_chips(x, y)
        id_x, id_y, id_diagonal = [2 * chip[0] + chip[1] for chip in (via_x, via_y, diagonal)]
        me, sibling = (x, y, c), (x, y, 1 - c)

        def rows_of(a, half, quarter):
            total = src[a].shape[0]
            if quarter is None:
                return _half_rows(total, half)
            return pl.ds(pl.multiple_of(half * (total // 2) + quarter * (total // 4), total // 4), total // 4)

        def copy(a, k, chip_index, half, quarter, to, from_src=False):
            rows = rows_of(a, half, quarter)
            dst = outs[a].at[chip_index, rows]
            return pltpu.make_async_remote_copy(
                src_ref=src[a].at[rows] if from_src else dst, dst_ref=dst, send_sem=send_sems.at[slots * a + k],
                recv_sem=recv_sems.at[slots * a + k], device_id=to, device_id_type=MESH)

        def whole_copy(b, k, chip_index, to):
            return pltpu.make_async_remote_copy(
                src_ref=src[n + b], dst_ref=outs[n + b].at[chip_index], send_sem=send_sems.at[slots * n + 3 * b + k],
                recv_sem=recv_sems.at[slots * n + 3 * b + k], device_id=to, device_id_type=MESH)

        first, stages, last = [], [], []
        for a in range(n):
            first += [copy(a, 0, 2 * x + y, c, None, (*via_x, c), True), copy(a, 1, 2 * x + y, c, None, (*via_y, c), True)]
            stages.append([
                (copy(a, 0, id_x, c, None, me),
                 [copy(a, 2, id_x, c, 0, (*via_y, c)), copy(a, 4, id_x, c, None, sibling)]),
                (copy(a, 1, id_y, c, None, me),
                 [copy(a, 3, id_y, c, 1, (*via_x, c)), copy(a, 5, id_y, c, None, sibling)]),
                (copy(a, 2, id_diagonal, c, 0, me), [copy(a, 6, id_diagonal, c, 0, sibling)]),
                (copy(a, 3, id_diagonal, c, 1, me), [copy(a, 7, id_diagonal, c, 1, sibling)]),
            ])
            last += [copy(a, 4, id_x, 1 - c, None, me), copy(a, 5, id_y, 1 - c, None, me),
                     copy(a, 6, id_diagonal, 1 - c, 0, me), copy(a, 7, id_diagonal, 1 - c, 1, me)]
        for b in range(nw):
            for k, (chip, index) in enumerate(((via_x, id_x), (via_y, id_y), (diagonal, id_diagonal))):
                first.append(whole_copy(b, k, 2 * x + y, (*chip, c)))
                last.append(whole_copy(b, k, index, me))
        return first, stages, last

    def start(*refs):
        for cp in plan(*refs)[0]:
            cp.start()

    def pass_on(stages, which):
        for stage in which:
            for per_shard in stages:
                lands, onward = per_shard[stage]
                lands.wait_recv()
                for cp in onward:
                    cp.start()

    def middle(*refs):
        pass_on(plan(*refs)[1], (0, 1))

    def rest(*refs):
        first, stages, last = plan(*refs)
        pass_on(stages, (2, 3))
        for cp in last:
            cp.wait_recv()
        for cp in first + [cp for per_shard in stages for _, onward in per_shard for cp in onward]:
            cp.wait_send()

    out_shape = [jax.ShapeDtypeStruct((NCHIP,) + s.shape, s.dtype) for s in list(shards) + list(whole)]
    return _Exchange(list(shards) + list(whole), out_shape, slots * n + 3 * nw, start, middle=middle, rest=rest)


def _with_own(gathered, own):
    x, y, _ = _place()
    return lax.dynamic_update_index_in_dim(gathered, own, 2 * x + y, axis=0)


def _simple_exchange(inputs, out_shape, copies_of):
    def start(*refs):
        for cp in copies_of(*refs):
            cp.start()

    def finish(*refs):
        for cp in copies_of(*refs):
            cp.wait()

    return _Exchange(list(inputs), out_shape, len(out_shape) * 3, start, finish)


def _pair_exchange(grads):
    def copies_of(src, outs, send_sems, recv_sems):
        x, y, c = _place()
        return [pltpu.make_async_remote_copy(
            src_ref=src[a].at[:, _half_rows(src[a].shape[1], 1 - c)], dst_ref=outs[a], send_sem=send_sems.at[a],
            recv_sem=recv_sems.at[a], device_id=(x, y, 1 - c), device_id_type=MESH) for a in range(len(src))]

    return _simple_exchange(
        grads, [jax.ShapeDtypeStruct((g.shape[0], g.shape[1] // 2, g.shape[2]), g.dtype) for g in grads], copies_of)


def _pair_sum(grads, theirs, name):
    n = len(grads)

    def body(*refs):
        south = lax.axis_index("c") == 0
        for a in range(n):
            g = refs[a][...]
            half = g.shape[0] // 2
            mine = jnp.where(south, g[:half], g[half:])
            refs[2 * n + a][...] = (mine.astype(F32) + refs[n + a][...].astype(F32)).astype(BF16)

    def specs(arrs):
        return [pl.BlockSpec((None,) + g.shape[1:], lambda j: (j, 0, 0)) for g in arrs]

    return pl.pallas_call(
        body, name=name, grid=(NCHIP,), in_specs=specs(grads) + specs(theirs), out_specs=specs(theirs),
        out_shape=[jax.ShapeDtypeStruct(g.shape, BF16) for g in theirs], compiler_params=_cparams(),
    )(*grads, *theirs)


def _chip_exchange(parts):
    def copies_of(src, outs, send_sems, recv_sems):
        x, y, c = _place()
        return [pltpu.make_async_remote_copy(
            src_ref=src[a].at[2 * chip[0] + chip[1]], dst_ref=outs[a].at[k], send_sem=send_sems.at[3 * a + k],
            recv_sem=recv_sems.at[3 * a + k], device_id=(*chip, c), device_id_type=MESH)
            for a in range(len(src)) for k, chip in enumerate(_other_chips(x, y))]

    return _simple_exchange(parts, [jax.ShapeDtypeStruct((NCHIP - 1,) + p.shape[1:], p.dtype) for p in parts], copies_of)


def _chip_sum(parts, received, exchange=None):
    n = len(parts)
    steps = 4
    ex_in, ex_in_specs, ex_out_specs, ex_out_shape, ex_scratch = _hosted(exchange)
    n_ex = len(ex_in)

    def body(*refs):
        mine, theirs = refs[2 * n + n_ex:3 * n + n_ex], refs[3 * n + n_ex:4 * n + n_ex]
        ex_refs = refs[2 * n:2 * n + n_ex] + refs[4 * n + n_ex:len(refs) - n - 2]
        tiles, send_sems, recv_sems = refs[len(refs) - n - 2:len(refs) - 2], refs[-2], refs[-1]
        step = pl.program_id(0)
        if exchange is not None:
            @pl.when(step == 0)
            def _():
                exchange.start(*exchange.split(ex_refs))

        x, y, c = _place()

        def share(a, i):
            rows = tiles[a].shape[1]
            return pltpu.make_async_remote_copy(
                src_ref=tiles[a].at[i], dst_ref=theirs[a].at[pl.ds(pl.multiple_of(i * rows, rows), rows)],
                send_sem=send_sems.at[a * steps + i], recv_sem=recv_sems.at[a * steps + i], device_id=(x, y, 1 - c),
                device_id_type=MESH)

        chip = 2 * x + y
        for a in range(n):
            p, r = refs[a], refs[n + a]
            own = jnp.where(chip == 0, p[0], jnp.where(chip == 1, p[1], jnp.where(chip == 2, p[2], p[3])))
            total = ((own.astype(F32) + r[0].astype(F32)) + r[1].astype(F32)) + r[2].astype(F32)
            mine[a][...] = total
            tiles[a][step] = total
            share(a, step).start()

        @pl.when(step == steps - 1)
        def _():
            if exchange is not None:
                exchange.finish(*exchange.split(ex_refs))
            for a in range(n):
                for i in range(steps):
                    share(a, i).wait()

    def specs(arrs):
        return [pl.BlockSpec((g.shape[0], g.shape[1] // steps, g.shape[2]), lambda i: (0, i, 0)) for g in arrs]

    out_specs = [pl.BlockSpec((g.shape[1] // steps, g.shape[2]), lambda i: (i, 0)) for g in parts]
    halves = [jax.ShapeDtypeStruct(g.shape[1:], F32) for g in parts]
    res = pl.pallas_call(
        body, name="grads_chip_sum", grid=(steps,), in_specs=specs(parts) + specs(received) + ex_in_specs,
        out_specs=out_specs + [HBM] * n + ex_out_specs, out_shape=halves * 2 + ex_out_shape,
        scratch_shapes=ex_scratch + [pltpu.VMEM((steps, g.shape[1] // steps, g.shape[2]), F32) for g in parts]
        + [pltpu.SemaphoreType.DMA((n * steps,))] * 2, compiler_params=_cparams(),
    )(*parts, *received, *ex_in)
    return res[:n], res[n:2 * n], res[2 * n:]


def _adamw_math(w, g, m, v):
    nm = ADAM_B1 * m + (1.0 - ADAM_B1) * g
    nv = ADAM_B2 * v + (1.0 - ADAM_B2) * jnp.square(g)
    m_hat = nm / (1.0 - ADAM_B1 ** ADAM_STEP)
    v_hat = nv / (1.0 - ADAM_B2 ** ADAM_STEP)
    return -ADAM_LR * (m_hat / (jnp.sqrt(v_hat) + ADAM_EPS) + ADAM_WD * w), nm, nv


def _adamw_big(ws, g_mine, g_theirs, ms, vs, exchange=None):
    n = len(ws)
    steps = 8
    ex_in, ex_in_specs, ex_out_specs, ex_out_shape, ex_scratch = _hosted(exchange)

    def body(*refs):
        ex_refs = refs[5 * n:5 * n + len(ex_in)] + refs[9 * n + len(ex_in):]
        outs = refs[5 * n + len(ex_in):9 * n + len(ex_in)]
        if exchange is not None:
            @pl.when(pl.program_id(0) == 0)
            def _():
                exchange.start(*exchange.split(ex_refs))

        own_half = (pl.program_id(0) // (steps // 2)) == lax.axis_index("c")
        for a in range(n):
            g = jnp.where(own_half, refs[n + a][...], refs[2 * n + a][...])
            d, nm, nv = _adamw_math(refs[a][...], g, refs[3 * n + a][...], refs[4 * n + a][...])
            outs[a][...] = g
            outs[n + a][...] = d
            outs[2 * n + a][...] = nm
            outs[3 * n + a][...] = nv

        if exchange is not None:
            @pl.when(pl.program_id(0) == steps - 1)
            def _():
                exchange.finish(*exchange.split(ex_refs))

    specs = [pl.BlockSpec((w.shape[0] // steps, w.shape[1]), lambda i: (i, 0)) for w in ws]
    half_specs = [pl.BlockSpec((g.shape[0] // (steps // 2), g.shape[1]), lambda i: (i % (steps // 2), 0)) for g in g_mine]
    shapes = [jax.ShapeDtypeStruct(w.shape, F32) for w in ws]
    res = pl.pallas_call(
        body, name="adamw_big", grid=(steps,), in_specs=specs + half_specs * 2 + specs * 2 + ex_in_specs,
        out_specs=specs * 4 + ex_out_specs, out_shape=shapes * 4 + ex_out_shape, scratch_shapes=ex_scratch,
        compiler_params=_cparams(),
    )(*ws, *g_mine, *g_theirs, *ms, *vs, *ex_in)
    return res[:n], res[n:2 * n], res[2 * n:3 * n], res[3 * n:4 * n], res[4 * n:]


def _adamw_in(w, g_mine, g_theirs, m, v):
    half = D // 2

    def body(w_ref, gm_ref, gt_ref, m_ref, v_ref, g_out, d_out, nm_out, nv_out, g_ref):
        south = lax.axis_index("c") == 0
        g_ref[0:half, :] = jnp.where(south, gm_ref[...], gt_ref[...])
        g_ref[half:D, :] = jnp.where(south, gt_ref[...], gm_ref[...])
        g = g_ref[0:CW, :]
        d, nm, nv = _adamw_math(w_ref[...], g, m_ref[...], v_ref[...])
        g_out[...] = g
        d_out[...] = d
        nm_out[...] = nm
        nv_out[...] = nv

    spec = pl.BlockSpec((CW, LANES), lambda i: (0, i))
    half_spec = pl.BlockSpec((half, LANES), lambda i: (0, i))
    return pl.pallas_call(
        body, name="adamw_in", grid=(D // LANES,), in_specs=[spec, half_spec, half_spec, spec, spec], out_specs=[spec] * 4,
        out_shape=[jax.ShapeDtypeStruct((CW, D), F32)] * 4, scratch_shapes=[pltpu.VMEM((D, LANES), F32)],
        compiler_params=_cparams(),
    )(w, g_mine, g_theirs, m, v)


NORM_NAMES = ("pre_mix_norm", "post_mix_norm", "pre_mlp_norm", "post_mlp_norm")
SMALL_NAMES = NORM_NAMES + ("gdn_conv_w", "fox_f_bias", "gdn_dt_bias", "gdn_a_log", "fox_out_norm", "gdn_out_norm")
CONV_COLS = 3 * DGDN // NCHIP


def _small_gather(d_norms, d_conv, sums, d_fox_norm, d_gdn_norm, loss_row):
    n_arrays = 6
    n_remote = n_arrays * (NDEV - 1)

    def copies_of(src, outs, send_sems, recv_sems):
        x, y, c = _place()
        me = 4 * x + 2 * y + c

        def from_me(chip_index):
            cols = pl.ds(pl.multiple_of(chip_index * CONV_COLS, LANES), CONV_COLS)
            return [src[0], src[1].at[:, cols], src[2], src[3], src[4], src[5]]

        local = [pltpu.make_async_copy(s, outs[a].at[me], send_sems.at[n_remote + a]) for a, s in enumerate(from_me(2 * x + y))]
        remote = []
        for k in range(1, NDEV):
            px, py, pc = x ^ ((k >> 2) & 1), y ^ ((k >> 1) & 1), c ^ (k & 1)
            remote += [pltpu.make_async_remote_copy(
                src_ref=s, dst_ref=outs[a].at[me], send_sem=send_sems.at[n_arrays * (k - 1) + a],
                recv_sem=recv_sems.at[n_arrays * (k - 1) + a], device_id=(px, py, pc), device_id_type=MESH)
                for a, s in enumerate(from_me(2 * px + py))]
        return local + remote

    def start(*refs):
        for cp in copies_of(*refs):
            cp.start()

    def finish(*refs):
        for cp in copies_of(*refs):
            cp.wait()

    shapes = [(4, D), (CONV_K, CONV_COLS), (8, LANES), (1, LANES), (1, LANES), (1, LANES)]
    return _Exchange([d_norms, d_conv, sums, d_fox_norm, d_gdn_norm, loss_row],
                     [jax.ShapeDtypeStruct((NDEV,) + s, F32) for s in shapes], n_remote + n_arrays, start, finish)


def _small_adamw(gathered, ws, ms, vs):
    n = len(SMALL_NAMES)
    ng = len(gathered)

    def body(*refs):
        def total(buf):
            acc = buf[0]
            for i in range(1, NDEV):
                acc = acc + buf[i]
            return acc

        t_norms, t_conv, t_sums, t_fn, t_gn, t_loss = [total(r) for r in refs[:ng]]
        w_refs, m_refs, v_refs = refs[ng:ng + n], refs[ng + n:ng + 2 * n], refs[ng + 2 * n:ng + 3 * n]
        outs = refs[ng + 3 * n:]
        outs[4 * n][...] = t_loss
        grads = [t_norms[i:i + 1, :] for i in range(4)] + [
            t_conv, t_sums[0:1, 0:NFH], t_sums[1:2, 0:NGH], t_sums[2:3, 0:NGH], t_fn[:, 0:FHD], t_gn]
        for a in range(n):
            d, nm, nv = _adamw_math(w_refs[a][...], grads[a], m_refs[a][...], v_refs[a][...])
            outs[a][...] = grads[a]
            outs[n + a][...] = d
            outs[2 * n + a][...] = nm
            outs[3 * n + a][...] = nv

    def whole(arr):
        return pl.BlockSpec(arr.shape, lambda i: (0,) * arr.ndim)

    res = pl.pallas_call(
        body, name="small_adamw", grid=(1,), in_specs=[whole(t) for t in gathered] + [whole(w) for w in ws] * 3,
        out_specs=[whole(w) for w in ws] * 4 + [pl.BlockSpec((1, LANES), lambda i: (0, 0))],
        out_shape=[jax.ShapeDtypeStruct(w.shape, F32) for w in ws] * 4 + [jax.ShapeDtypeStruct((1, LANES), F32)],
        compiler_params=_cparams(),
    )(*gathered, *ws, *ms, *vs)
    return res[:n], res[n:2 * n], res[2 * n:3 * n], res[3 * n:4 * n], res[4 * n]


CW = DPROJ // NCHIP
PROJ_RUNS = tuple((part * DFOX + hp * LANES, part * DFOX + (hp + 1) * LANES, (3 * hp + part) * LANES)
                  for hp in range(NPAIR) for part in range(3)) + (
    (1536, 1544, BLK_SMALL * LANES), (1544, 3080, BLK_GDN * LANES), (3080, 3088, BLK_SMALL * LANES + 8),
    (3088, 3600, BLK_GZ * LANES))


def _proj_pieces():
    pieces = []
    for lo, hi, at in PROJ_RUNS:
        while lo < hi:
            j = lo // CW
            end = min(hi, (j + 1) * CW)
            pieces.append((j, lo - j * CW, at, end - lo))
            at, lo = at + end - lo, end
    return pieces


RT = 256


def _to_padded_rows(gathered):
    def body(src_ref, out_ref, blocks_ref, rows_ref):
        blocks_ref[...] = src_ref[...].astype(F32)
        rows_ref[...] = jnp.zeros_like(rows_ref)
        for j, start, at, n in _proj_pieces():
            rows_ref[at:at + n, :] = blocks_ref[j, start:start + n, :]
        out_ref[...] = rows_ref[...].astype(out_ref.dtype)

    return pl.pallas_call(
        body, name="proj_rows_in", grid=(D // RT,), in_specs=[pl.BlockSpec((NCHIP, D, RT), lambda i: (0, 0, i))],
        out_specs=pl.BlockSpec((DPROJ_PAD, RT), lambda i: (0, i)), out_shape=jax.ShapeDtypeStruct((DPROJ_PAD, D), gathered.dtype),
        scratch_shapes=[pltpu.VMEM((NCHIP, D, RT), F32), pltpu.VMEM((DPROJ_PAD, RT), F32)], compiler_params=_cparams(),
    )(gathered)


def _from_padded_rows(w):
    def body(src_ref, out_ref, rows_ref, blocks_ref):
        rows_ref[...] = src_ref[...].astype(F32)
        blocks_ref[...] = jnp.zeros_like(blocks_ref)
        for j, start, at, n in _proj_pieces():
            blocks_ref[j, start:start + n, :] = rows_ref[at:at + n, :]
        out_ref[...] = blocks_ref[...].astype(out_ref.dtype)

    return pl.pallas_call(
        body, name="proj_rows_out", grid=(D // RT,), in_specs=[pl.BlockSpec((DPROJ_PAD, RT), lambda i: (0, i))],
        out_specs=pl.BlockSpec((NCHIP, D, RT), lambda i: (0, 0, i)), out_shape=jax.ShapeDtypeStruct((NCHIP, D, D), w.dtype),
        scratch_shapes=[pltpu.VMEM((DPROJ_PAD, RT), F32), pltpu.VMEM((NCHIP, D, RT), F32)], compiler_params=_cparams(),
    )(w)


def _local_step(x, target, first_weights, late_weights, reduce_late, reduce_in, pre_mix_norm, fox_f_bias, fox_out_norm,
                gdn_a_log, gdn_dt_bias, gdn_out_norm, post_mix_norm, pre_mlp_norm, post_mlp_norm):
    bias_vec = jnp.zeros((1, LANES), F32).at[0, 0:NFH].set(fox_f_bias).at[0, LANE_G:LANE_G + NGH].set(gdn_dt_bias)
    alog_vec = jnp.zeros((1, LANES), F32).at[0, LANE_G:LANE_G + NGH].set(gdn_a_log)
    w2 = jnp.concatenate([fox_out_norm, fox_out_norm], axis=1)

    h, first = _pre_norm(x, pre_mix_norm, exchange=first_weights[0])
    win_p, conv_w = first_weights[1](first)
    proj = _matmul(h, win_p, tb=True, tm=2048, tn=768, tk=1024, name="mm_proj", exchange=late_weights[0])
    proj, late_a = proj if late_weights[0] is not None else (proj, [])
    gates = _gates(proj, bias_vec, alog_vec)
    mix, fox_o, lse, late_b = _fox_fwd(proj, gates, w2, exchange=late_weights[1])
    qkv = _gdn_pre(proj, conv_w)
    (u, w, qd, kd, a_intra, gl, t_inv), _ = _gdn_prep(qkv, gates)
    wout, wup3 = late_weights[3](late_a, late_b)
    mix, gdn_raw, states = _gdn_scan(u, w, qd, kd, a_intra, gl, proj, gdn_out_norm, mix)
    def post_mix(acc, xv, w_post, w_pre_mlp):
        x1v = xv + acc * _rms_scale(acc) * w_post
        return acc, x1v, x1v * _rms_scale(x1v) * w_pre_mlp

    mixed, x1, h2 = _matmul(mix, wout, tm=512, tn=D, tk=1024, out_dtypes=(F32, F32, BF16), name="mm_out",
                            extra=(x, post_mix_norm, pre_mlp_norm), epilogue=post_mix)

    def relu2(acc):
        r = jnp.maximum(acc, 0.0)
        return r, r * r

    up_act = _matmul(h2, wup3, b3=True, tm=1024, tn=1024, tk=1024, out_dtypes=(BF16, BF16), epilogue=relu2,
                     name="mm_up", exchange=late_weights[2])
    (up_relu, act), late_c = up_act if late_weights[2] is not None else (up_act, [])
    wdown = late_weights[4](late_c)
    def loss_head(acc, x1v, tv, w):
        err = x1v + acc * _rms_scale(acc) * w - tv
        dx2v = err * (1.0 / D)
        dyv, dwt = _rms_bwd(acc, w, dx2v)
        part = 0.5 * jnp.sum(jnp.mean(err * err, axis=-1, keepdims=True), axis=0, keepdims=True)
        return dx2v, dyv, jnp.sum(dwt, axis=0, keepdims=True), jnp.broadcast_to(part, (1, D))

    dx2, dy, d_post_mlp, loss_wide = _matmul(
        act, wdown, tm=512, tn=D, tk=DFF, out_dtypes=(F32, BF16, F32, F32), extra=(x1, target, post_mlp_norm),
        epilogue=loss_head, n_sums=2, name="mm_down")
    loss_row = loss_wide[:, :LANES]

    dwdown = _matmul(act, dy, ta=True, tm=1024, tn=1024, tk=2048, out_dtypes=(BF16,), name="mm_dwdown")

    def relu2_bwd(acc, r):
        return (acc * 2.0 * r.astype(F32),)

    dup = _matmul(dy, wdown, tb=True, tm=1024, tn=1024, tk=1024, out_dtypes=(BF16,), extra=(up_relu,), epilogue=relu2_bwd,
                  name="mm_dact")
    dwup3 = _matmul(h2, dup, ta=True, tm=1024, tn=1024, tk=2048, out_dtypes=(BF16,), o3=True, name="mm_dwup")
    def mid_bwd(acc, x1v, dx2v, mixedv, w_pre_mlp, w_post):
        dxa, dwm = _rms_bwd(x1v, w_pre_mlp, acc)
        dx1v = dx2v + dxa
        dm, dwp = _rms_bwd(mixedv, w_post, dx1v)
        return dx1v, dm, jnp.sum(dwm, axis=0, keepdims=True), jnp.sum(dwp, axis=0, keepdims=True)

    dx1, dmixed, d_pre_mlp, d_post_mix = _matmul(
        dup, wup3, tb=True, b3=True, tm=512, tn=D, tk=DFF, out_dtypes=(F32, BF16, F32, F32),
        extra=(x1, dx2, mixed, pre_mlp_norm, post_mix_norm), epilogue=mid_bwd, n_sums=2, name="mm_dh2")
    dwout = _matmul(mix, dmixed, ta=True, tm=1024, tn=1024, tk=2048, out_dtypes=(BF16,), name="mm_dwout")
    dmix = _matmul(dmixed, wout, tb=True, tm=2048, tk=1024, name="mm_dmix")

    dfox, delta, d_fox_norm, from_sibling = _fox_norm_bwd(fox_o, dmix, w2, exchange=reduce_late[0](dwout, dwup3, dwdown))
    dproj, dcum_fox, reduced_a = _fox_bwd(proj, dfox, gates, lse, delta, exchange=reduce_late[1](from_sibling))
    (dproj, du, dw, dqd, dkd, da, dgl, d_gdn_norm), reduced_b = _gdn_scan_bwd(
        dmix, gdn_raw, proj, gdn_out_norm, u, w, qd, kd, a_intra, gl, states, dproj, exchange=reduce_late[2]())
    dqkv, dgates_gdn, reduced_c = _gdn_prep_bwd(qkv, gates, t_inv, du, dw, dqd, dkd, da, dgl, exchange=reduce_late[3]())
    reduced_late = (reduced_a, reduced_b, reduced_c)
    dproj, d_conv = _gdn_pre_bwd(proj, conv_w, dqkv, dproj)
    dproj, sums = _gates_bwd(proj, bias_vec, alog_vec, dgates_gdn, dcum_fox, dproj)

    dwin_p = _matmul(dproj, h, ta=True, tm=1280, tn=1024, tk=2048, out_dtypes=(BF16,), name="mm_dwin")
    exchange_in = reduce_in(dwin_p)
    dh = _matmul(dproj, win_p, tm=1024, tk=DPROJ_PAD, name="mm_dh", exchange=exchange_in)
    dh, reduced_in = dh if exchange_in is not None else (dh, [])
    grad_x, d_pre_mix = _pre_norm_bwd(dh, x, pre_mix_norm, dx1)

    d_norms = jnp.concatenate([d_pre_mix, d_post_mix, d_pre_mlp, d_post_mlp], axis=0)
    return grad_x, (d_norms, d_conv, sums, d_fox_norm, d_gdn_norm, loss_row), reduced_late, reduced_in


def kernel(x, pre_mix_norm, w_in, fox_f_bias, fox_out_norm, gdn_conv_w, gdn_a_log, gdn_dt_bias, gdn_out_norm, w_out, post_mix_norm, pre_mlp_norm, w_up, w_down, post_mlp_norm, loss_target, m_pre_mix_norm, m_w_in, m_fox_f_bias, m_fox_out_norm, m_gdn_conv_w, m_gdn_a_log, m_gdn_dt_bias, m_gdn_out_norm, m_w_out, m_post_mix_norm, m_pre_mlp_norm, m_w_up, m_w_down, m_post_mlp_norm, v_pre_mix_norm, v_w_in, v_fox_f_bias, v_fox_out_norm, v_gdn_conv_w, v_gdn_a_log, v_gdn_dt_bias, v_gdn_out_norm, v_w_out, v_post_mix_norm, v_pre_mlp_norm, v_w_up, v_w_down, v_post_mlp_norm):
    weights = dict(pre_mix_norm=pre_mix_norm, w_in=w_in, fox_f_bias=fox_f_bias, fox_out_norm=fox_out_norm, gdn_conv_w=gdn_conv_w,
                   gdn_a_log=gdn_a_log, gdn_dt_bias=gdn_dt_bias, gdn_out_norm=gdn_out_norm, w_out=w_out, post_mix_norm=post_mix_norm,
                   pre_mlp_norm=pre_mlp_norm, w_up=w_up, w_down=w_down, post_mlp_norm=post_mlp_norm)
    m_in = dict(pre_mix_norm=m_pre_mix_norm, w_in=m_w_in, fox_f_bias=m_fox_f_bias, fox_out_norm=m_fox_out_norm, gdn_conv_w=m_gdn_conv_w,
                gdn_a_log=m_gdn_a_log, gdn_dt_bias=m_gdn_dt_bias, gdn_out_norm=m_gdn_out_norm, w_out=m_w_out, post_mix_norm=m_post_mix_norm,
                pre_mlp_norm=m_pre_mlp_norm, w_up=m_w_up, w_down=m_w_down, post_mlp_norm=m_post_mlp_norm)
    v_in = dict(pre_mix_norm=v_pre_mix_norm, w_in=v_w_in, fox_f_bias=v_fox_f_bias, fox_out_norm=v_fox_out_norm, gdn_conv_w=v_gdn_conv_w,
                gdn_a_log=v_gdn_a_log, gdn_dt_bias=v_gdn_dt_bias, gdn_out_norm=v_gdn_out_norm, w_out=v_w_out, post_mix_norm=v_post_mix_norm,
                pre_mlp_norm=v_pre_mlp_norm, w_up=v_w_up, w_down=v_w_down, post_mlp_norm=v_post_mlp_norm)
    order_w = ("pre_mix_norm", "w_in", "fox_f_bias", "fox_out_norm", "gdn_conv_w", "gdn_a_log", "gdn_dt_bias", "gdn_out_norm", "w_out",
               "post_mix_norm", "pre_mlp_norm", "w_up", "w_down", "post_mlp_norm")
    big = ("w_in", "w_out", "w_up", "w_down")

    def row(v):
        return v if v.ndim == 2 else v.reshape(1, -1)

    win_shard = jnp.pad(w_in.T.astype(BF16), ((0, D - CW), (0, 0)))

    def resolve_first(gathered):
        win_g, conv_g = gathered
        return (_to_padded_rows(_with_own(win_g, win_shard)),
                _with_own(conv_g, gdn_conv_w).transpose(1, 0, 2).reshape(CONV_K, 3 * DGDN))

    late_shards = [weights[n].astype(BF16) for n in big[1:]]

    gathered_down = []

    def resolve_out_up(gathered_out, gathered_mlp):
        gathered_down.append(gathered_mlp[1])
        return _with_own(gathered_out[0], late_shards[0]).reshape(D, D), _with_own(gathered_mlp[0], late_shards[1])

    def resolve_down(_):
        return _with_own(gathered_down[0], late_shards[2]).reshape(DFF, D)

    pair_sums, late_blocks = {}, []

    def pair_summed(names, blocks, theirs):
        for n, s in zip(names, _pair_sum(blocks, theirs, "grads_pair_sum_" + names[0])):
            pair_sums[n] = s

    def late_pair_exchange(dwout, dwup3, dwdown):
        late_blocks.extend([dwout.reshape(NCHIP, D // NCHIP, D), dwup3, dwdown.reshape(NCHIP, DFF // NCHIP, D)])
        return _pair_exchange(late_blocks)

    def late_chip_exchange(theirs):
        pair_summed(big[1:], late_blocks, theirs)
        return _chip_exchange([pair_sums["w_up"], pair_sums["w_down"]])

    def reduce_in(dwin_p):
        blocks = [_from_padded_rows(dwin_p)]
        pair_summed(big[:1], blocks, _run_exchange(_pair_exchange(blocks), "grads_pair_exchange_w_in"))
        return _chip_exchange([pair_sums["w_in"]])

    grad_x, small, received_late, received_in = _local_step(
        x[0], loss_target[0], (_allgather_exchange([win_shard], whole=[gdn_conv_w]), resolve_first),
        (_allgather_exchange(late_shards[:1]), _allgather_exchange(late_shards[1:]), None, resolve_out_up, resolve_down),
        (late_pair_exchange, late_chip_exchange, lambda: None, lambda: _chip_exchange([pair_sums["w_out"]])),
        reduce_in, row(pre_mix_norm), fox_f_bias, row(fox_out_norm), gdn_a_log, gdn_dt_bias,
        row(gdn_out_norm), row(post_mix_norm), row(pre_mlp_norm), row(post_mlp_norm))
    received_mlp, _, received_out = received_late

    g_mine, g_theirs, small_gathered = _chip_sum(
        [pair_sums[n] for n in big], list(received_in[:1]) + list(received_out[:1]) + list(received_mlp[:2]),
        exchange=_small_gather(*small))

    g_big, d_big, nm_big, nv_big, _ = _adamw_big(
        [weights[n] for n in big[1:]], g_mine[1:], g_theirs[1:], [m_in[n] for n in big[1:]], [v_in[n] for n in big[1:]])
    in_t = _adamw_in(w_in.T, g_mine[0], g_theirs[0], m_w_in.T, v_w_in.T)
    g_small, d_small, nm_small, nv_small, loss_total = _small_adamw(
        small_gathered, [row(weights[n]) for n in SMALL_NAMES], [row(m_in[n]) for n in SMALL_NAMES],
        [row(v_in[n]) for n in SMALL_NAMES])

    grads, delta, new_m, new_v = {}, {}, {}, {}
    grads["w_in"], delta["w_in"], new_m["w_in"], new_v["w_in"] = [t.T for t in in_t]
    for i, n in enumerate(big[1:]):
        grads[n], delta[n], new_m[n], new_v[n] = g_big[i], d_big[i], nm_big[i], nv_big[i]
    for i, n in enumerate(SMALL_NAMES):
        shape = weights[n].shape
        grads[n], delta[n], new_m[n], new_v[n] = (g_small[i].reshape(shape), d_small[i].reshape(shape),
                                                  nm_small[i].reshape(shape), nv_small[i].reshape(shape))
    return (loss_total[0, 0], grad_x[None], *[grads[n] for n in order_w], *[delta[n] for n in order_w], *[new_m[n] for n in order_w],
            *[new_v[n] for n in order_w])
```

```python
import jax
import jax.numpy as jnp
from jax import lax
from jax.experimental import pallas as pl
from jax.experimental.pallas import tpu as pltpu

F32 = jnp.float32
BF16 = jnp.bfloat16
MESH = pl.DeviceIdType.MESH

S = 2048
D = 1024
NFH, FHD = 8, 64
NPAIR = NFH // 2
NGH, GHD = 4, 128
DFOX = NFH * FHD
DGDN = NGH * GHD
CHUNK = 64
NCH = S // CHUNK
CONV_K = 4
DFF = 4 * D
EPS = 1e-6
DPROJ = 3600
LANES = 128
DPROJ_PAD = 3840
BLK_GDN = 12
BLK_GZ = 24
BLK_SMALL = 28
NCHIP = 4
NDEV = 8
VMEM_LIMIT = 56 * 1024 * 1024

ADAM_LR = 0.001
ADAM_B1 = 0.9
ADAM_B2 = 0.999
ADAM_EPS = 1e-08
ADAM_WD = 0.01
ADAM_STEP = 10


def _cparams(**kw):
    return pltpu.CompilerParams(vmem_limit_bytes=VMEM_LIMIT, **kw)


def _dn(ca, cb):
    return (((ca,), (cb,)), ((), ()))


def _dot(a, b, ca=1, cb=0):
    return lax.dot_general(a.astype(BF16), b.astype(BF16), _dn(ca, cb), preferred_element_type=F32)


def _hdot(a, b, ca=1, cb=0):
    return lax.dot_general(a.astype(F32), b.astype(F32), _dn(ca, cb), precision=lax.Precision.HIGHEST,
                           preferred_element_type=F32)


def _dot3(a, b, ca=1, cb=0):
    a_hi, b_hi = a.astype(BF16), b.astype(BF16)
    a_lo, b_lo = (a - a_hi.astype(F32)).astype(BF16), (b - b_hi.astype(F32)).astype(BF16)
    dn = _dn(ca, cb)
    return (lax.dot_general(a_hi, b_hi, dn, preferred_element_type=F32)
            + (lax.dot_general(a_hi, b_lo, dn, preferred_element_type=F32)
               + lax.dot_general(a_lo, b_hi, dn, preferred_element_type=F32)))


@jax.custom_vjp
def _mm_nn(a, b):
    return _dot(a, b, 1, 0)


def _mm_nn_fwd(a, b):
    return _dot(a, b, 1, 0), (a, b)


def _mm_nn_bwd(res, g):
    a, b = res
    return _dot(g, b, 1, 1), _dot(a, g, 0, 0)


_mm_nn.defvjp(_mm_nn_fwd, _mm_nn_bwd)


@jax.custom_vjp
def _mm_nt(a, b):
    return _dot(a, b, 1, 1)


def _mm_nt_fwd(a, b):
    return _dot(a, b, 1, 1), (a, b)


def _mm_nt_bwd(res, g):
    a, b = res
    return _dot(g, b, 1, 0), _dot(g, a, 0, 0)


_mm_nt.defvjp(_mm_nt_fwd, _mm_nt_bwd)


@jax.custom_vjp
def _saved_inverse(m, t_inv):
    del m
    return t_inv


def _saved_inverse_fwd(m, t_inv):
    del m
    return t_inv, t_inv


def _saved_inverse_bwd(t_inv, g):
    return -_dot3(_dot3(t_inv, g, 0, 0), t_inv, 1, 1), jnp.zeros_like(t_inv)


_saved_inverse.defvjp(_saved_inverse_fwd, _saved_inverse_bwd)


def _sigmoid(z):
    return 1.0 / (1.0 + jnp.exp(-z))


def _softplus(z):
    return jnp.maximum(z, 0.0) + jnp.log(1.0 + jnp.exp(-jnp.abs(z)))


def _silu(z):
    return z * _sigmoid(z)


def _rms_scale(x):
    return lax.rsqrt(jnp.mean(x * x, axis=-1, keepdims=True) + EPS)


def _rms_bwd(x, w, g):
    r = _rms_scale(x)
    gw = g * w
    dx = r * gw - x * (r * r * r) * jnp.mean(gw * x, axis=-1, keepdims=True)
    return dx, g * x * r


def _matmul(a, b, *, name, ta=False, tb=False, tm=512, tn=512, tk=512, out_dtypes=(F32,), b3=False, o3=False,
            extra=(), epilogue=None, exchange=None, n_sums=0):
    m, k = (a.shape[1], a.shape[0]) if ta else a.shape
    if b3:
        n = b.shape[1] if tb else b.shape[0] * b.shape[2]
        kb = b.shape[0] * b.shape[2] if tb else b.shape[1]
    else:
        n, kb = (b.shape[0], b.shape[1]) if tb else (b.shape[1], b.shape[0])
    assert kb == k, (name, kb, k)
    tm, tn, tk = min(tm, m), min(tn, n), min(tk, k)
    assert m % tm == 0 and n % tn == 0 and k % tk == 0, (name, m, n, k, tm, tn, tk)
    nk = k // tk
    whole_k_blocks = b3 and tb and not ta and nk == 1 and b.shape[0] > 1
    n_extra = len(extra)
    n_out = len(out_dtypes)
    grid = (m // tm, n // tn, nk)
    ex_in, ex_in_specs, ex_out_specs, ex_out_shape, ex_scratch = _hosted(exchange)

    def body(*refs):
        a_ref, b_ref = refs[0], refs[1]
        extra_refs = refs[2:2 + n_extra]
        first_out = 2 + n_extra + len(ex_in)
        out_refs = refs[first_out:first_out + n_out]
        ex_refs = refs[2 + n_extra:first_out] + refs[first_out + n_out:first_out + n_out + len(ex_out_shape)] + refs[-2:]
        step = [pl.program_id(d) for d in range(3)]

        if exchange is not None:
            @pl.when((step[0] == 0) & (step[1] == 0) & (step[2] == 0))
            def _():
                exchange.start(*exchange.split(ex_refs))

        def finish(acc):
            outs = (acc,) if epilogue is None else epilogue(acc, *[r[...] for r in extra_refs])
            for o_ref, val in zip(out_refs[:n_out - n_sums], outs):
                o_ref[...] = val.astype(o_ref.dtype)
            for o_ref, val in zip(out_refs[n_out - n_sums:], outs[n_out - n_sums:]):
                @pl.when(step[0] == 0)
                def _(o_ref=o_ref, val=val):
                    o_ref[...] = val

                @pl.when(step[0] > 0)
                def _(o_ref=o_ref, val=val):
                    o_ref[...] += val

        if whole_k_blocks:
            width = b.shape[2]
            part = _dot(a_ref[:, 0:width], b_ref[0], 1, 1)
            for blk in range(1, b.shape[0]):
                part = part + _dot(a_ref[:, blk * width:(blk + 1) * width], b_ref[blk], 1, 1)
        else:
            part = _dot(a_ref[...], b_ref[...], 0 if ta else 1, 1 if tb else 0)
        if nk == 1:
            finish(part)
        else:
            acc_ref = refs[first_out + n_out + len(ex_out_shape)]

            @pl.when(step[2] == 0)
            def _():
                acc_ref[...] = part

            @pl.when(step[2] > 0)
            def _():
                acc_ref[...] += part

            @pl.when(step[2] == nk - 1)
            def _():
                finish(acc_ref[...])

        if exchange is not None:
            flat = (step[0] * grid[1] + step[1]) * nk + step[2]
            total = grid[0] * grid[1] * nk

            @pl.when(flat == total // 2)
            def _():
                exchange.middle(*exchange.split(ex_refs))

            @pl.when(flat == total - 1)
            def _():
                exchange.rest(*exchange.split(ex_refs))

    a_spec = pl.BlockSpec((tk, tm), lambda i, j, kk: (kk, i)) if ta else pl.BlockSpec((tm, tk), lambda i, j, kk: (i, kk))
    if whole_k_blocks:
        b_spec = pl.BlockSpec((b.shape[0], tn, b.shape[2]), lambda i, j, kk: (0, j, 0))
    elif b3 and tb:
        assert b.shape[2] == tk
        b_spec = pl.BlockSpec((None, tn, tk), lambda i, j, kk: (kk, j, 0))
    elif b3:
        assert b.shape[2] == tn
        b_spec = pl.BlockSpec((None, tk, tn), lambda i, j, kk: (j, kk, 0))
    elif tb:
        b_spec = pl.BlockSpec((tn, tk), lambda i, j, kk: (j, kk))
    else:
        b_spec = pl.BlockSpec((tk, tn), lambda i, j, kk: (kk, j))
    tile = pl.BlockSpec((tm, tn), lambda i, j, kk: (i, j))
    out_specs = [tile] * n_out
    out_shape = [jax.ShapeDtypeStruct((m, n), dt) for dt in out_dtypes]
    if o3:
        out_specs[0] = pl.BlockSpec((None, tm, tn), lambda i, j, kk: (j, i, 0))
        out_shape[0] = jax.ShapeDtypeStruct((n // tn, m, tn), out_dtypes[0])
    assert n_sums == 0 or tn == n
    for r in range(n_out - n_sums, n_out):
        out_specs[r] = pl.BlockSpec((1, tn), lambda i, j, kk: (0, 0))
        out_shape[r] = jax.ShapeDtypeStruct((1, n), out_dtypes[r])
    res = pl.pallas_call(
        body, name=name, grid=grid,
        in_specs=[a_spec, b_spec] + [tile if e.shape[0] == m else pl.BlockSpec((1, tn), lambda i, j, kk: (0, j)) for e in extra]
        + ex_in_specs, out_specs=out_specs + ex_out_specs,
        out_shape=out_shape + ex_out_shape,
        scratch_shapes=([pltpu.VMEM((tm, tn), F32)] if nk > 1 else []) + ex_scratch,
        compiler_params=_cparams(),
    )(a, b, *extra, *ex_in)
    if exchange is not None:
        return (res[0] if n_out == 1 else res[:n_out]), res[n_out:]
    return res[0] if n_out == 1 else res


TR = 256


def _row_spec(cols):
    return pl.BlockSpec((TR, cols), lambda i: (i, 0))


def _vec_spec(cols):
    return pl.BlockSpec((1, cols), lambda i: (0, 0))


def _pre_norm(x, w, exchange=None):
    ex_in, ex_in_specs, ex_out_specs, ex_out_shape, ex_scratch = _hosted(exchange)

    def body(*refs):
        x_ref, w_ref, h_ref = refs[0], refs[1], refs[2 + len(ex_in)]
        ex_refs = refs[2:2 + len(ex_in)] + refs[3 + len(ex_in):]
        if exchange is not None:
            @pl.when(pl.program_id(0) == 0)
            def _():
                exchange.start(*exchange.split(ex_refs))

        xv = x_ref[...]
        h_ref[...] = (xv * _rms_scale(xv) * w_ref[...]).astype(BF16)

        if exchange is not None:
            @pl.when(pl.program_id(0) == S // TR - 1)
            def _():
                exchange.finish(*exchange.split(ex_refs))

    res = pl.pallas_call(
        body, name="pre_norm", grid=(S // TR,), in_specs=[_row_spec(D), _vec_spec(D)] + ex_in_specs,
        out_specs=[_row_spec(D)] + ex_out_specs, out_shape=[jax.ShapeDtypeStruct((S, D), BF16)] + ex_out_shape,
        scratch_shapes=ex_scratch, compiler_params=_cparams(),
    )(x, w, *ex_in)
    return res[0], res[1:]


def _pre_norm_bwd(dh, x, w, dx1):
    def body(dh_ref, x_ref, w_ref, dx1_ref, dx_ref, dw_ref):
        i = pl.program_id(0)
        dxa, dwt = _rms_bwd(x_ref[...], w_ref[...], dh_ref[...])
        dx_ref[...] = dx1_ref[...] + dxa

        @pl.when(i == 0)
        def _():
            dw_ref[...] = jnp.zeros_like(dw_ref)

        dw_ref[...] += jnp.sum(dwt, axis=0, keepdims=True)

    return pl.pallas_call(
        body, name="pre_norm_bwd", grid=(S // TR,),
        in_specs=[_row_spec(D), _row_spec(D), _vec_spec(D), _row_spec(D)], out_specs=[_row_spec(D), _vec_spec(D)],
        out_shape=[jax.ShapeDtypeStruct((S, D), F32), jax.ShapeDtypeStruct((1, D), F32)], compiler_params=_cparams(),
    )(dh, x, w, dx1)


BQ = 512
NQ = S // BQ
LANE_BETA, LANE_G = 8, 12


def _gate_lanes(shape):
    lane = lax.broadcasted_iota(jnp.int32, shape, 1)
    return lane < LANE_BETA, (lane >= LANE_BETA) & (lane < LANE_G), (lane >= LANE_G) & (lane < LANE_G + NGH)


def _gates(proj, bias_vec, alog_vec):
    def body(s_ref, b_ref, a_ref, o_ref, carry_ref):
        i = pl.program_id(0)

        @pl.when(i == 0)
        def _():
            carry_ref[...] = jnp.zeros_like(carry_ref)

        z = s_ref[...] + b_ref[...]
        tail = jnp.log(1.0 + jnp.exp(-jnp.abs(z)))
        sp = jnp.maximum(z, 0.0) + tail
        lf = jnp.minimum(z, 0.0) - tail
        r = lax.broadcasted_iota(jnp.int32, (BQ, BQ), 0)
        c = lax.broadcasted_iota(jnp.int32, (BQ, BQ), 1)
        tri = (c <= r).astype(F32)
        cum = _hdot(tri, lf) + carry_ref[...]
        carry_ref[...] = cum[BQ - 1:BQ, :]
        is_fox, is_beta, is_g = _gate_lanes(z.shape)
        o_ref[...] = jnp.where(is_fox, cum, jnp.where(is_beta, _sigmoid(z), jnp.where(is_g, -jnp.exp(a_ref[...]) * sp, 0.0)))

    return pl.pallas_call(
        body, name="gates", grid=(NQ,),
        in_specs=[pl.BlockSpec((BQ, LANES), lambda i: (i, BLK_SMALL)), _vec_spec(LANES), _vec_spec(LANES)],
        out_specs=pl.BlockSpec((BQ, LANES), lambda i: (i, 0)), out_shape=jax.ShapeDtypeStruct((S, LANES), F32),
        scratch_shapes=[pltpu.VMEM((1, LANES), F32)], compiler_params=_cparams(),
    )(proj, bias_vec, alog_vec)


def _gates_bwd(proj, bias_vec, alog_vec, dgates_gdn, dcum_fox, dproj):
    def body(s_ref, b_ref, a_ref, dg_ref, dc_ref, dproj_in, dproj_ref, red_ref, carry_ref):
        del dproj_in
        i = pl.program_id(0)

        @pl.when(i == 0)
        def _():
            carry_ref[...] = jnp.zeros_like(carry_ref)
            red_ref[...] = jnp.zeros_like(red_ref)

        z = s_ref[...] + b_ref[...]
        dg = dg_ref[...] + dc_ref[...]
        r = lax.broadcasted_iota(jnp.int32, (BQ, BQ), 0)
        c = lax.broadcasted_iota(jnp.int32, (BQ, BQ), 1)
        upper = (c >= r).astype(F32)
        dlf = _hdot(upper, dg) + carry_ref[...]
        carry_ref[...] = dlf[0:1, :]
        sig = _sigmoid(z)
        g_scale = -jnp.exp(a_ref[...])
        is_fox, is_beta, is_g = _gate_lanes(z.shape)
        ds = jnp.where(is_fox, dlf * (1.0 - sig), jnp.where(is_beta, dg * sig * (1.0 - sig), jnp.where(is_g, dg * g_scale * sig, 0.0)))
        dproj_ref[:, 0:LANES] = ds.astype(BF16)
        dproj_ref[:, LANES:2 * LANES] = jnp.zeros((BQ, LANES), BF16)
        dalog = jnp.where(is_g, dg * g_scale * _softplus(z), 0.0)
        sums = jnp.sum(ds, axis=0, keepdims=True)
        red_ref[0:1, :] += jnp.where(is_fox[0:1], sums, 0.0)
        red_ref[1:2, :] += pltpu.roll(jnp.where(is_g[0:1], sums, 0.0), LANES - LANE_G, 1)
        red_ref[2:3, :] += pltpu.roll(jnp.sum(dalog, axis=0, keepdims=True), LANES - LANE_G, 1)

    blk = pl.BlockSpec((BQ, LANES), lambda i: (NQ - 1 - i, 0))
    return pl.pallas_call(
        body, name="gates_bwd", grid=(NQ,),
        in_specs=[pl.BlockSpec((BQ, LANES), lambda i: (NQ - 1 - i, BLK_SMALL)), _vec_spec(LANES), _vec_spec(LANES), blk, blk,
                  pl.BlockSpec(memory_space=pl.ANY)],
        out_specs=[pl.BlockSpec((BQ, 2 * LANES), lambda i: (NQ - 1 - i, BLK_SMALL // 2)), pl.BlockSpec((8, LANES), lambda i: (0, 0))],
        out_shape=[jax.ShapeDtypeStruct((S, DPROJ_PAD), BF16), jax.ShapeDtypeStruct((8, LANES), F32)],
        input_output_aliases={5: 0},
        scratch_shapes=[pltpu.VMEM((1, LANES), F32)], compiler_params=_cparams(),
    )(proj, bias_vec, alog_vec, dgates_gdn, dcum_fox, dproj)


FOX_SCALE = FHD ** -0.5
FOX_PAIRS = 2
FOX_PAIRS_BWD = 2


def _head_mask(e):
    lane = lax.broadcasted_iota(jnp.int32, (1, LANES), 1)
    return (lane >= e * FHD) & (lane < (e + 1) * FHD)


def _lane_col(vals, index):
    lane = lax.broadcasted_iota(jnp.int32, vals.shape, 1)
    return jnp.sum(jnp.where(lane == index, vals, 0.0), axis=1, keepdims=True)


def _sublane_row(vals, index):
    row = lax.broadcasted_iota(jnp.int32, vals.shape, 0)
    return jnp.sum(jnp.where(row == index, vals, 0.0), axis=0, keepdims=True)


def _pair_cols(c0, c1):
    lane = lax.broadcasted_iota(jnp.int32, (c0.shape[0], 2), 1)
    return jnp.where(lane == 0, c0, c1)


def _split3(x):
    hi = x.astype(BF16).astype(F32)
    rest = x - hi
    mid = rest.astype(BF16).astype(F32)
    return hi, mid, (rest - mid).astype(BF16).astype(F32)


def _fox_operand(vals, e, cum, is_query):
    lane = lax.broadcasted_iota(jnp.int32, (1, LANES), 1)
    base = (1 - e) * FHD
    parts = _split3(cum)
    own = jnp.where(_head_mask(e), vals * FOX_SCALE if is_query else vals, 0.0)
    cum_at, ones_at = (base, base + 3) if is_query else (base + 3, base)
    sign = 1.0 if is_query else -1.0
    out = own + jnp.where((lane >= ones_at) & (lane < ones_at + 3), 1.0, 0.0)
    for i, part in enumerate(parts):
        out = out + jnp.where(lane == cum_at + i, sign * part, 0.0)
    return out.astype(BF16)


def _causal_block():
    return lax.broadcasted_iota(jnp.int32, (BQ, BQ), 1) <= lax.broadcasted_iota(jnp.int32, (BQ, BQ), 0)


def _head_rms(o, masks):
    o2 = o * o
    r = [lax.rsqrt(jnp.sum(jnp.where(mk, o2, 0.0), axis=1, keepdims=True) * (1.0 / FHD) + EPS) for mk in masks]
    return jnp.where(masks[0], r[0], r[1])


def _hosted(exchange):
    if exchange is None:
        return [], [], [], [], []
    return (exchange.inputs, [HBM] * len(exchange.inputs), [HBM] * len(exchange.out_shape), exchange.out_shape,
            exchange.sem_shapes())


def _fox_fwd(proj, gates, w2, exchange=None):
    ex_in, ex_in_specs, ex_out_specs, ex_out_shape, ex_scratch = _hosted(exchange)

    n_in = 3 * FOX_PAIRS + 2
    heads = [(pp, e) for pp in range(FOX_PAIRS) for e in range(2)]

    def body(*refs):
        qkv_refs, g_ref, w_ref = refs[:3 * FOX_PAIRS], refs[3 * FOX_PAIRS], refs[3 * FOX_PAIRS + 1]
        mix_ref, o_ref, lse_ref = refs[n_in + len(ex_in):n_in + 3 + len(ex_in)]
        ka_ref, vb_ref = refs[n_in + 3 + len(ex_in) + len(ex_out_shape):n_in + 5 + len(ex_in) + len(ex_out_shape)]
        ex_refs = refs[n_in:n_in + len(ex_in)] + refs[n_in + 3 + len(ex_in):n_in + 3 + len(ex_in) + len(ex_out_shape)] + refs[-2:]
        grp, qi = pl.program_id(0), pl.program_id(1)

        def head_index(pp, e):
            return 2 * (FOX_PAIRS * grp + pp) + e

        if exchange is not None:
            @pl.when((grp == 0) & (qi == 0))
            def _():
                exchange.start(*exchange.split(ex_refs))

        @pl.when(qi == 0)
        def _():
            gt = g_ref[...]
            for pp in range(FOX_PAIRS):
                kv = qkv_refs[3 * pp + 1][...]
                for e in range(2):
                    ka_ref[2 * pp + e] = _fox_operand(kv, e, _lane_col(gt, head_index(pp, e)), False)
                vb_ref[pp] = qkv_refs[3 * pp + 2][...].astype(BF16)

        masks = [_head_mask(0), _head_mask(1)]
        gt = g_ref[pl.ds(pl.multiple_of(qi * BQ, BQ), BQ), :]
        qs = [_fox_operand(qkv_refs[3 * pp][...], e, _lane_col(gt, head_index(pp, e)), True) for pp, e in heads]
        n = range(len(heads))

        def block(kj, carry, diagonal):
            rows = pl.ds(pl.multiple_of(kj * BQ, BQ), BQ)
            s = [_dot(qs[i], ka_ref[i, rows, :], 1, 1) for i in n]
            if diagonal:
                s = [jnp.where(_causal_block(), s[i], -jnp.inf) for i in n]
            m_new = [jnp.maximum(carry[i][0], jnp.max(s[i], axis=-1, keepdims=True)) for i in n]
            p = [jnp.exp(s[i] - m_new[i]) for i in n]
            alpha = [jnp.exp(carry[i][0] - m_new[i]) for i in n]
            l_new = [alpha[i] * carry[i][1] + jnp.sum(p[i], axis=-1, keepdims=True) for i in n]
            pv = [_dot(p[i], vb_ref[heads[i][0], rows, :]) for i in n]
            return tuple((m_new[i], l_new[i], alpha[i] * carry[i][2] + pv[i]) for i in n)

        one = (jnp.full((BQ, 1), -jnp.inf, F32), jnp.zeros((BQ, 1), F32), jnp.zeros((BQ, LANES), F32))
        below = lax.fori_loop(0, qi, lambda kj, carry: block(kj, carry, False), (one,) * len(heads))
        done = block(qi, below, True)
        for pp in range(FOX_PAIRS):
            (m0, l0, a0), (m1, l1, a1) = done[2 * pp], done[2 * pp + 1]
            o = jnp.where(masks[0], a0 / l0, a1 / l1)
            cols = slice(pp * LANES, (pp + 1) * LANES)
            o_ref[:, cols] = o
            mix_ref[:, cols] = (o * _head_rms(o, masks) * w_ref[...]).astype(BF16)
            lse_ref[pp] = _pair_cols(m0 + jnp.log(l0), m1 + jnp.log(l1))

        if exchange is not None:
            @pl.when((grp == NPAIR // FOX_PAIRS // 2) & (qi == 0))
            def _():
                exchange.middle(*exchange.split(ex_refs))

            @pl.when((grp == NPAIR // FOX_PAIRS - 1) & (qi == NQ - 1))
            def _():
                exchange.rest(*exchange.split(ex_refs))

    qkv_specs = []
    for pp in range(FOX_PAIRS):
        qkv_specs.append(pl.BlockSpec((BQ, LANES), lambda g, i, pp=pp: (i, 3 * (FOX_PAIRS * g + pp))))
        qkv_specs.append(pl.BlockSpec((S, LANES), lambda g, i, pp=pp: (0, 3 * (FOX_PAIRS * g + pp) + 1)))
        qkv_specs.append(pl.BlockSpec((S, LANES), lambda g, i, pp=pp: (0, 3 * (FOX_PAIRS * g + pp) + 2)))
    blk = pl.BlockSpec((BQ, FOX_PAIRS * LANES), lambda g, i: (i, g))
    res = pl.pallas_call(
        body, name="fox_fwd", grid=(NPAIR // FOX_PAIRS, NQ),
        in_specs=qkv_specs + [pl.BlockSpec((S, LANES), lambda g, i: (0, 0)), pl.BlockSpec((1, LANES), lambda g, i: (0, 0))]
        + ex_in_specs,
        out_specs=[blk, blk, pl.BlockSpec((FOX_PAIRS, BQ, 2), lambda g, i: (g, i, 0))] + ex_out_specs,
        out_shape=[jax.ShapeDtypeStruct((S, D), BF16), jax.ShapeDtypeStruct((S, DFOX), F32),
                   jax.ShapeDtypeStruct((NPAIR, S, 2), F32)] + ex_out_shape,
        scratch_shapes=[pltpu.VMEM((2 * FOX_PAIRS, S, LANES), BF16), pltpu.VMEM((FOX_PAIRS, S, LANES), BF16)] + ex_scratch,
        compiler_params=_cparams(),
    )(*([proj] * (3 * FOX_PAIRS)), gates, w2, *ex_in)
    return res[0], res[1], res[2], res[3:]


def _fox_norm_bwd(o, dmix, w2, exchange=None):
    ex_in, ex_in_specs, ex_out_specs, ex_out_shape, ex_scratch = _hosted(exchange)

    def body(*refs):
        o_ref, g_ref, w_ref = refs[:3]
        do_ref, dl_ref, dw_ref = refs[3 + len(ex_in):6 + len(ex_in)]
        ex_refs = refs[3:3 + len(ex_in)] + refs[6 + len(ex_in):]
        hp, qi = pl.program_id(0), pl.program_id(1)

        if exchange is not None:
            @pl.when((hp == 0) & (qi == 0))
            def _():
                exchange.start(*exchange.split(ex_refs))

        masks = [_head_mask(0), _head_mask(1)]
        ov = o_ref[...]
        g = g_ref[...]
        r = _head_rms(ov, masks)
        gw = g * w_ref[...]
        gwo = gw * ov
        mean = [jnp.sum(jnp.where(mk, gwo, 0.0), axis=1, keepdims=True) * (1.0 / FHD) for mk in masks]
        do = r * gw - ov * (r * r * r) * jnp.where(masks[0], mean[0], mean[1])
        do_ref[...] = do.astype(BF16)
        doo = do * ov
        dl_ref[...] = _pair_cols(*[jnp.sum(jnp.where(mk, doo, 0.0), axis=1, keepdims=True) for mk in masks])

        @pl.when((hp == 0) & (qi == 0))
        def _():
            dw_ref[...] = jnp.zeros_like(dw_ref)

        dw_ref[...] += jnp.sum(g * ov * r, axis=0, keepdims=True)

        @pl.when((hp == NPAIR - 1) & (qi == NQ - 1))
        def _():
            dw = dw_ref[...]
            dw_ref[...] = dw + pltpu.roll(dw, FHD, 1)
            if exchange is not None:
                exchange.finish(*exchange.split(ex_refs))

    blk = pl.BlockSpec((BQ, LANES), lambda hp, i: (i, hp))
    vec = pl.BlockSpec((1, LANES), lambda hp, i: (0, 0))
    res = pl.pallas_call(
        body, name="fox_norm_bwd", grid=(NPAIR, NQ), in_specs=[blk, blk, vec] + ex_in_specs,
        out_specs=[blk, pl.BlockSpec((None, BQ, 2), lambda hp, i: (hp, i, 0)), vec] + ex_out_specs,
        out_shape=[jax.ShapeDtypeStruct((S, DFOX), BF16), jax.ShapeDtypeStruct((NPAIR, S, 2), F32),
                   jax.ShapeDtypeStruct((1, LANES), F32)] + ex_out_shape,
        scratch_shapes=ex_scratch, compiler_params=_cparams(),
    )(o, dmix, w2, *ex_in)
    return res[0], res[1], res[2], res[3:]


def _fox_bwd(proj, do, gates, lse, delta, exchange=None):
    ex_in, ex_in_specs, ex_out_specs, ex_out_shape, ex_scratch = _hosted(exchange)

    pg = FOX_PAIRS_BWD
    n_in = 3 * pg + 4
    heads = [(pp, e) for pp in range(pg) for e in range(2)]

    def body(*refs):
        qkv_refs = refs[:3 * pg]
        do_ref, g_ref, lse_ref, dl_ref = refs[3 * pg:n_in]
        dproj_ref, dc_ref = refs[n_in + len(ex_in):n_in + 2 + len(ex_in)]
        qa_ref, dq_ref = refs[n_in + 2 + len(ex_in) + len(ex_out_shape):n_in + 4 + len(ex_in) + len(ex_out_shape)]
        ex_refs = refs[n_in:n_in + len(ex_in)] + refs[n_in + 2 + len(ex_in):n_in + 2 + len(ex_in) + len(ex_out_shape)] + refs[-2:]
        grp, kj = pl.program_id(0), pl.program_id(1)

        def head_index(pp, e):
            return 2 * (pg * grp + pp) + e

        if exchange is not None:
            @pl.when((grp == 0) & (kj == 0))
            def _():
                exchange.start(*exchange.split(ex_refs))

        @pl.when(kj == 0)
        def _():
            gt = g_ref[...]
            for pp in range(pg):
                qv = qkv_refs[3 * pp][...]
                for e in range(2):
                    qa_ref[2 * pp + e] = _fox_operand(qv, e, _lane_col(gt, head_index(pp, e)), True)
            dq_ref[...] = jnp.zeros_like(dq_ref)

        @pl.when((grp == 0) & (kj == 0))
        def _():
            dc_ref[...] = jnp.zeros_like(dc_ref)

        masks = [_head_mask(0), _head_mask(1)]
        krows = pl.ds(pl.multiple_of(kj * BQ, BQ), BQ)
        gk = g_ref[krows, :]
        kas = [_fox_operand(qkv_refs[3 * pp + 1][...], e, _lane_col(gk, head_index(pp, e)), False) for pp, e in heads]
        vbs = [qkv_refs[3 * pp + 2][...].astype(BF16) for pp in range(pg)]
        lane = lax.broadcasted_iota(jnp.int32, (BQ, LANES), 1)
        n = range(len(heads))

        def block(qi, carry, diagonal):
            dks, dvs, css = carry
            rows = pl.ds(pl.multiple_of(qi * BQ, BQ), BQ)
            qa = [qa_ref[i, rows, :] for i in n]
            s = [_dot(qa[i], kas[i], 1, 1) for i in n]
            if diagonal:
                s = [jnp.where(_causal_block(), s[i], -jnp.inf) for i in n]
            dov = [do_ref[rows, pp * LANES:(pp + 1) * LANES] for pp in range(pg)]
            doe = [jnp.where(masks[e], dov[pp], jnp.zeros_like(dov[pp])) for pp, e in heads]
            lse2 = [lse_ref[pp, rows, :] for pp in range(pg)]
            dl2 = [dl_ref[pp, rows, :] for pp in range(pg)]
            p = [jnp.exp(s[i] - _lane_col(lse2[heads[i][0]], heads[i][1])) for i in n]
            dp = [_dot(doe[i], vbs[heads[i][0]], 1, 1) for i in n]
            ds = [p[i] * (dp[i] - _lane_col(dl2[heads[i][0]], heads[i][1])) for i in n]
            dv_part = [_dot(p[i], doe[i], 0, 0) for i in n]
            dk_part = [_dot(ds[i], jnp.where(masks[heads[i][1]], qa[i], jnp.zeros_like(qa[i])), 0, 0) for i in n]
            dq_part = [jnp.where(masks[heads[i][1]], _dot(ds[i], kas[i]), 0.0) for i in n]
            css = tuple(css[i] + jnp.sum(ds[i], axis=0, keepdims=True) for i in n)
            dc = jnp.zeros((BQ, LANES), F32)
            for i in n:
                dc = dc + jnp.where(lane == head_index(*heads[i]), jnp.sum(ds[i], axis=1, keepdims=True), 0.0)
            for pp in range(pg):
                dq_ref[pp, rows, :] += (dq_part[2 * pp] + dq_part[2 * pp + 1]) * FOX_SCALE
            dc_ref[rows, :] += dc
            dks = tuple(dks[pp] + dk_part[2 * pp] + dk_part[2 * pp + 1] for pp in range(pg))
            dvs = tuple(dvs[pp] + dv_part[2 * pp] + dv_part[2 * pp + 1] for pp in range(pg))
            return dks, dvs, css

        zero = jnp.zeros((BQ, LANES), F32)
        first = block(kj, ((zero,) * pg, (zero,) * pg, (jnp.zeros((1, BQ), F32),) * len(heads)), True)
        dks, dvs, css = lax.fori_loop(kj + 1, NQ, lambda qi, carry: block(qi, carry, False), first)
        r = lax.broadcasted_iota(jnp.int32, (BQ, BQ), 0)
        c = lax.broadcasted_iota(jnp.int32, (BQ, BQ), 1)
        dcol = jnp.zeros((BQ, LANES), F32)
        for i in n:
            col = jnp.sum(jnp.where(r == c, css[i], 0.0), axis=1, keepdims=True)
            dcol = dcol + jnp.where(lane == head_index(*heads[i]), col, 0.0)
        dc_ref[krows, :] -= dcol
        for pp in range(pg):
            base = 3 * pp * LANES
            dproj_ref[krows, base + LANES:base + 2 * LANES] = dks[pp].astype(BF16)
            dproj_ref[krows, base + 2 * LANES:base + 3 * LANES] = dvs[pp].astype(BF16)

        @pl.when(kj == NQ - 1)
        def _():
            for pp in range(pg):
                dproj_ref[:, 3 * pp * LANES:(3 * pp + 1) * LANES] = dq_ref[pp].astype(BF16)

        if exchange is not None:
            @pl.when((grp == NPAIR // pg // 2) & (kj == 0))
            def _():
                exchange.middle(*exchange.split(ex_refs))

            @pl.when((grp == NPAIR // pg - 1) & (kj == NQ - 1))
            def _():
                exchange.rest(*exchange.split(ex_refs))

    qkv_specs = []
    for pp in range(pg):
        qkv_specs.append(pl.BlockSpec((S, LANES), lambda g, j, pp=pp: (0, 3 * (pg * g + pp))))
        qkv_specs.append(pl.BlockSpec((BQ, LANES), lambda g, j, pp=pp: (j, 3 * (pg * g + pp) + 1)))
        qkv_specs.append(pl.BlockSpec((BQ, LANES), lambda g, j, pp=pp: (j, 3 * (pg * g + pp) + 2)))
    pair = pl.BlockSpec((pg, S, 2), lambda g, j: (g, 0, 0))
    res = pl.pallas_call(
        body, name="fox_bwd", grid=(NPAIR // pg, NQ),
        in_specs=qkv_specs + [pl.BlockSpec((S, pg * LANES), lambda g, j: (0, g)), pl.BlockSpec((S, LANES), lambda g, j: (0, 0)),
                              pair, pair] + ex_in_specs,
        out_specs=[pl.BlockSpec((S, 3 * pg * LANES), lambda g, j: (0, g)), pl.BlockSpec((S, LANES), lambda g, j: (0, 0))]
        + ex_out_specs,
        out_shape=[jax.ShapeDtypeStruct((S, DPROJ_PAD), BF16), jax.ShapeDtypeStruct((S, LANES), F32)] + ex_out_shape,
        scratch_shapes=[pltpu.VMEM((2 * pg, S, LANES), BF16), pltpu.VMEM((pg, S, LANES), F32)] + ex_scratch,
        compiler_params=_cparams(),
    )(*([proj] * (3 * pg)), do, gates, lse, delta, *ex_in)
    return res[0], res[1], res[2:]


NQKV = 3 * NGH
GDN_QSCALE = GHD ** -0.5


def _shift_down(x, s):
    if s == 0:
        return x
    row = lax.broadcasted_iota(jnp.int32, x.shape, 0)
    return jnp.where(row >= s, pltpu.roll(x, s, 0), 0.0)


def _shift_up(x, s):
    if s == 0:
        return x
    n = x.shape[0]
    row = lax.broadcasted_iota(jnp.int32, x.shape, 0)
    return jnp.where(row < n - s, pltpu.roll(x, n - s, 0), 0.0)


def _conv_taps(xv):
    return [_shift_down(xv, CONV_K - 1 - j) for j in range(CONV_K)]


def _conv_pre(taps, wv):
    pre = taps[CONV_K - 1] * wv[CONV_K - 1:CONV_K, :]
    for j in range(CONV_K - 1):
        pre = pre + taps[j] * wv[j:j + 1, :]
    return pre


def _l2_factors(b):
    return b < 2 * NGH, jnp.where(b < NGH, GDN_QSCALE, 1.0)


def _gdn_pre(proj, conv_w):
    def body(x_ref, w_ref, o_ref):
        b = pl.program_id(0)
        c = _silu(_conv_pre(_conv_taps(x_ref[...]), w_ref[...]))
        normed, scale = _l2_factors(b)
        rs = lax.rsqrt(jnp.sum(c * c, axis=-1, keepdims=True) + EPS)
        o_ref[...] = c * jnp.where(normed, rs, 1.0) * scale

    return pl.pallas_call(
        body, name="gdn_pre", grid=(NQKV,),
        in_specs=[pl.BlockSpec((S, GHD), lambda b: (0, BLK_GDN + b)), pl.BlockSpec((CONV_K, GHD), lambda b: (0, b))],
        out_specs=pl.BlockSpec((S, GHD), lambda b: (0, b)),
        out_shape=jax.ShapeDtypeStruct((S, NQKV * GHD), F32), compiler_params=_cparams(),
    )(proj, conv_w)


def _gdn_pre_bwd(proj, conv_w, dqkv, dproj):
    def body(x_ref, w_ref, dy_ref, dproj_in, dx_ref, dw_ref):
        del dproj_in
        b = pl.program_id(0)
        taps = _conv_taps(x_ref[...])
        wv = w_ref[...]
        pre = _conv_pre(taps, wv)
        sig = _sigmoid(pre)
        c = pre * sig
        normed, scale = _l2_factors(b)
        g = dy_ref[...] * scale
        rs = lax.rsqrt(jnp.sum(c * c, axis=-1, keepdims=True) + EPS)
        dc_n = rs * g - c * (rs * rs * rs) * jnp.sum(g * c, axis=-1, keepdims=True)
        dc = jnp.where(normed, dc_n, g)
        dpre = dc * sig * (1.0 + pre * (1.0 - sig))
        dx = dpre * wv[CONV_K - 1:CONV_K, :]
        for j in range(CONV_K - 1):
            dx = dx + _shift_up(dpre, CONV_K - 1 - j) * wv[j:j + 1, :]
        dx_ref[...] = dx.astype(BF16)
        for j in range(CONV_K):
            dw_ref[j:j + 1, :] = jnp.sum(dpre * taps[j], axis=0, keepdims=True)

    return pl.pallas_call(
        body, name="gdn_pre_bwd", grid=(NQKV,),
        in_specs=[pl.BlockSpec((S, GHD), lambda b: (0, BLK_GDN + b)), pl.BlockSpec((CONV_K, GHD), lambda b: (0, b)),
                  pl.BlockSpec((None, S, GHD), lambda b: (b // NGH, 0, b % NGH)), pl.BlockSpec(memory_space=pl.ANY)],
        out_specs=[pl.BlockSpec((S, GHD), lambda b: (0, BLK_GDN + b)), pl.BlockSpec((CONV_K, GHD), lambda b: (0, b))],
        out_shape=[jax.ShapeDtypeStruct((S, DPROJ_PAD), BF16), jax.ShapeDtypeStruct((CONV_K, NQKV * GHD), F32)],
        input_output_aliases={3: 0}, compiler_params=_cparams(),
    )(proj, conv_w, dqkv, dproj)


CB = 16
NCB = NCH // CB


def _chunk_prep(qs, ks, vs, gcols, bcols, t_saved=None):
    n = range(len(qs))
    r = lax.broadcasted_iota(jnp.int32, (CHUNK, CHUNK), 0)
    c = lax.broadcasted_iota(jnp.int32, (CHUNK, CHUNK), 1)
    incl = c <= r
    eye = (r == c).astype(F32)
    grow = [jnp.sum(gcols[i] * eye, axis=0, keepdims=True) for i in n]
    gc_col = [jnp.sum(jnp.where(incl, grow[i], 0.0), axis=1, keepdims=True) for i in n]
    gc_row = [jnp.sum(jnp.where(r <= c, gcols[i], 0.0), axis=0, keepdims=True) for i in n]
    decay = [jnp.exp(jnp.where(incl, gc_col[i] - gc_row[i], -jnp.inf)) for i in n]
    kb = [ks[i] * bcols[i] for i in n]
    vb = [vs[i] * bcols[i] for i in n]
    kk = [_mm_nt(kb[i], ks[i]) for i in n]
    m = [jnp.where(c < r, kk[i] * decay[i], 0.0) for i in n]
    if t_saved is None:
        t_inv = [eye - m[i] for i in n]
        p = [_dot3(m[i], m[i]) for i in n]
        for step in range(5):
            t_inv = [t_inv[i] + _dot3(t_inv[i], p[i]) for i in n]
            if step < 4:
                p = [_dot3(p[i], p[i]) for i in n]
    else:
        t_inv = [_saved_inverse(m[i], t_saved[i]) for i in n]
    egc = [jnp.exp(gc_col[i]) for i in n]
    u = [_mm_nn(t_inv[i], vb[i]) for i in n]
    w = [_mm_nn(t_inv[i], kb[i] * egc[i]) for i in n]
    qk = [_mm_nt(qs[i], ks[i]) for i in n]
    gc_last = [gc_col[i][CHUNK - 1:CHUNK, :] for i in n]
    return [(u[i], w[i], qk[i] * decay[i], qs[i] * egc[i], ks[i] * jnp.exp(gc_last[i] - gc_col[i]), jnp.exp(gc_last[i]),
             t_inv[i]) for i in n]


def _prep_specs():
    rows = CB * CHUNK
    qs = pl.BlockSpec((rows, GHD), lambda i, h: (i, h))
    ks = pl.BlockSpec((rows, GHD), lambda i, h: (i, NGH + h))
    vs = pl.BlockSpec((rows, GHD), lambda i, h: (i, 2 * NGH + h))
    gs = pl.BlockSpec((rows, LANES), lambda i, h: (i, 0))
    a_s = pl.BlockSpec((None, rows, CHUNK), lambda i, h: (h, i, 0))
    gl_s = pl.BlockSpec((None, CB, 1, LANES), lambda i, h: (h, i, 0, 0))
    return qs, ks, vs, gs, a_s, gl_s


def _gdn_prep(qkv, gates, exchange=None):
    ex_in, ex_in_specs, ex_out_specs, ex_out_shape, ex_scratch = _hosted(exchange)

    def body(*refs):
        q_ref, k_ref, v_ref, g_ref = refs[:4]
        u_ref, w_ref, qd_ref, kd_ref, a_ref, gl_ref, t_ref = refs[4 + len(ex_in):11 + len(ex_in)]
        ex_refs = refs[4:4 + len(ex_in)] + refs[11 + len(ex_in):]
        h = pl.program_id(1)

        if exchange is not None:
            @pl.when((pl.program_id(0) == 0) & (h == 0))
            def _():
                exchange.start(*exchange.split(ex_refs))

        chunks = [pl.ds(cidx * CHUNK, CHUNK) for cidx in range(CB)]
        gts = [g_ref[rows, :] for rows in chunks]
        outs = _chunk_prep([q_ref[rows, :] for rows in chunks], [k_ref[rows, :] for rows in chunks],
                           [v_ref[rows, :] for rows in chunks], [_lane_col(gt, LANE_G + h) for gt in gts],
                           [_lane_col(gt, LANE_BETA + h) for gt in gts])
        for cidx, rows in enumerate(chunks):
            u, w, a, qd, kd, gl, t_inv = outs[cidx]
            u_ref[rows, :] = u
            w_ref[rows, :] = w
            qd_ref[rows, :] = qd
            kd_ref[rows, :] = kd
            a_ref[rows, :] = a
            t_ref[rows, :] = t_inv
            gl_ref[cidx] = jnp.broadcast_to(gl, (1, LANES))

        if exchange is not None:
            @pl.when((pl.program_id(0) == NCB // 2) & (h == 0))
            def _():
                exchange.middle(*exchange.split(ex_refs))

            @pl.when((pl.program_id(0) == NCB - 1) & (h == NGH - 1))
            def _():
                exchange.rest(*exchange.split(ex_refs))

    qs, ks, vs, gs, a_s, gl_s = _prep_specs()
    tok = jax.ShapeDtypeStruct((S, DGDN), F32)
    sq = jax.ShapeDtypeStruct((NGH, S, CHUNK), F32)
    res = pl.pallas_call(
        body, name="gdn_prep", grid=(NCB, NGH), in_specs=[qs, ks, vs, gs] + ex_in_specs,
        out_specs=[qs, qs, qs, qs, a_s, gl_s, a_s] + ex_out_specs,
        out_shape=[tok, tok, tok, tok, sq, jax.ShapeDtypeStruct((NGH, NCH, 1, LANES), F32), sq] + ex_out_shape,
        scratch_shapes=ex_scratch, compiler_params=_cparams(),
    )(qkv, qkv, qkv, gates, *ex_in)
    return res[:7], res[7:]


def _gdn_prep_bwd(qkv, gates, t_inv, du, dw, dqd, dkd, da, dgl, exchange=None):
    ex_in, ex_in_specs, ex_out_specs, ex_out_shape, ex_scratch = _hosted(exchange)

    def body(*refs):
        q_ref, k_ref, v_ref, g_ref, t_ref, du_ref, dw_ref, dqd_ref, dkd_ref, da_ref, dgl_ref = refs[:11]
        dqkv_ref, dg_ref = refs[11 + len(ex_in):13 + len(ex_in)]
        ex_refs = refs[11:11 + len(ex_in)] + refs[13 + len(ex_in):]
        h = pl.program_id(1)

        if exchange is not None:
            @pl.when((pl.program_id(0) == 0) & (h == 0))
            def _():
                exchange.start(*exchange.split(ex_refs))

        @pl.when(h == 0)
        def _():
            dg_ref[...] = jnp.zeros_like(dg_ref)

        lane = lax.broadcasted_iota(jnp.int32, (CHUNK, LANES), 1)
        chunks = [pl.ds(cidx * CHUNK, CHUNK) for cidx in range(CB)]
        gts = [g_ref[rows, :] for rows in chunks]
        t_saved = [t_ref[rows, :] for rows in chunks]
        _, vjp = jax.vjp(lambda *args: [o[:6] for o in _chunk_prep(*args, t_saved=t_saved)],
                         [q_ref[rows, :] for rows in chunks], [k_ref[rows, :] for rows in chunks],
                         [v_ref[rows, :] for rows in chunks], [_lane_col(gt, LANE_G + h) for gt in gts],
                         [_lane_col(gt, LANE_BETA + h) for gt in gts])
        dqs, dks, dvs, dgcs, dbcs = vjp([(du_ref[rows, :], dw_ref[rows, :], da_ref[rows, :], dqd_ref[rows, :],
                                          dkd_ref[rows, :], dgl_ref[cidx][:, 0:1]) for cidx, rows in enumerate(chunks)])
        for cidx, rows in enumerate(chunks):
            dq, dk, dv, dgc, dbc = dqs[cidx], dks[cidx], dvs[cidx], dgcs[cidx], dbcs[cidx]
            dqkv_ref[0, rows, :] = dq
            dqkv_ref[1, rows, :] = dk
            dqkv_ref[2, rows, :] = dv
            dg_ref[rows, :] += jnp.where(lane == LANE_G + h, dgc, 0.0) + jnp.where(lane == LANE_BETA + h, dbc, 0.0)

        if exchange is not None:
            @pl.when((pl.program_id(0) == NCB - 1) & (h == NGH - 1))
            def _():
                exchange.finish(*exchange.split(ex_refs))

    qs, ks, vs, gs, a_s, gl_s = _prep_specs()
    res = pl.pallas_call(
        body, name="gdn_prep_bwd", grid=(NCB, NGH), in_specs=[qs, ks, vs, gs, a_s, qs, qs, qs, qs, a_s, gl_s] + ex_in_specs,
        out_specs=[pl.BlockSpec((3, CB * CHUNK, GHD), lambda i, h: (0, i, h)), gs] + ex_out_specs,
        out_shape=[jax.ShapeDtypeStruct((3, S, DGDN), F32), jax.ShapeDtypeStruct((S, LANES), F32)] + ex_out_shape,
        scratch_shapes=ex_scratch, compiler_params=_cparams(),
    )(qkv, qkv, qkv, gates, t_inv, du, dw, dqd, dkd, da, dgl, *ex_in)
    return res[0], res[1], res[2:]


def _scan_specs(nh, parts, reverse):
    wide, rows, chunks = nh * GHD, S // parts, NCH // parts

    def part(p):
        return parts - 1 - p if reverse else p

    hs = pl.BlockSpec((rows, wide), lambda g, p: (part(p), g))
    a_s = pl.BlockSpec((nh, rows, CHUNK), lambda g, p: (g, part(p), 0))
    gl_s = pl.BlockSpec((nh, chunks, 1, LANES), lambda g, p: (g, part(p), 0, 0))
    st_s = pl.BlockSpec((nh, chunks, GHD, GHD), lambda g, p: (g, part(p), 0, 0))
    gz_s = pl.BlockSpec((rows, wide), lambda g, p: (part(p), BLK_GZ // nh + g))
    mix_s = pl.BlockSpec((rows, wide), lambda g, p: (part(p), NPAIR // nh + g))
    return hs, a_s, gl_s, st_s, gz_s, mix_s


def _head_cols(hh):
    return slice(hh * GHD, (hh + 1) * GHD)


SCAN_HEADS, SCAN_PARTS = 4, 2
SCAN_HEADS_BWD, SCAN_PARTS_BWD = 4, 4


def _gdn_scan(u, w, qd, kd, a, gl, proj, w_norm, mix):
    heads = range(SCAN_HEADS)

    def body(u_ref, w_ref, qd_ref, kd_ref, a_ref, gl_ref, z_ref, wn_ref, mix_in, mix_ref, o_ref, st_ref, carry_ref):
        del mix_in

        @pl.when(pl.program_id(1) == 0)
        def _():
            carry_ref[...] = jnp.zeros_like(carry_ref)

        def step(ci, states):
            rows = pl.ds(pl.multiple_of(ci * CHUNK, CHUNK), CHUNK)
            for hh in heads:
                st_ref[hh, ci] = states[hh]
            ws = [_dot(w_ref[rows, _head_cols(hh)], states[hh]) for hh in heads]
            qs = [_dot(qd_ref[rows, _head_cols(hh)], states[hh]) for hh in heads]
            vn = [u_ref[rows, _head_cols(hh)] - ws[hh] for hh in heads]
            av = [_dot(a_ref[hh, rows, :], vn[hh]) for hh in heads]
            kv = [_dot(kd_ref[rows, _head_cols(hh)], vn[hh], 0, 0) for hh in heads]
            for hh in heads:
                o_ref[rows, _head_cols(hh)] = qs[hh] + av[hh]
            return tuple(states[hh] * gl_ref[hh, ci] + kv[hh] for hh in heads)

        last = lax.fori_loop(0, NCH // SCAN_PARTS, step, tuple(carry_ref[hh] for hh in heads))
        for hh in heads:
            carry_ref[hh] = last[hh]
            ov = o_ref[:, _head_cols(hh)]
            mix_ref[:, _head_cols(hh)] = (ov * _rms_scale(ov) * wn_ref[...] * _silu(z_ref[:, _head_cols(hh)])).astype(BF16)

    hs, a_s, gl_s, st_s, gz_s, mix_s = _scan_specs(SCAN_HEADS, SCAN_PARTS, False)
    return pl.pallas_call(
        body, name="gdn_scan", grid=(NGH // SCAN_HEADS, SCAN_PARTS),
        in_specs=[hs, hs, hs, hs, a_s, gl_s, gz_s, pl.BlockSpec((1, GHD), lambda g, p: (0, 0)),
                  pl.BlockSpec(memory_space=pl.ANY)],
        out_specs=[mix_s, hs, st_s],
        out_shape=[jax.ShapeDtypeStruct((S, D), BF16), jax.ShapeDtypeStruct((S, DGDN), F32),
                   jax.ShapeDtypeStruct((NGH, NCH, GHD, GHD), F32)],
        input_output_aliases={8: 0}, scratch_shapes=[pltpu.VMEM((SCAN_HEADS, GHD, GHD), F32)], compiler_params=_cparams(),
    )(u, w, qd, kd, a, gl, proj, w_norm, mix)


def _gdn_scan_bwd(dmix, o, proj, w_norm, u, w, qd, kd, a, gl, states, dproj, exchange=None):
    ex_in, ex_in_specs, ex_out_specs, ex_out_shape, ex_scratch = _hosted(exchange)
    groups = NGH // SCAN_HEADS_BWD

    def body(*refs):
        dy_ref, o_ref, z_ref, wn_ref, u_ref, w_ref, qd_ref, kd_ref, a_ref, gl_ref, st_ref = refs[:11]
        dz_ref, du_ref, dw_ref, dqd_ref, dkd_ref, da_ref, dgl_ref, dwn_ref = refs[12 + len(ex_in):20 + len(ex_in)]
        do_ref, carry_ref = refs[20 + len(ex_in) + len(ex_out_shape):22 + len(ex_in) + len(ex_out_shape)]
        ex_refs = refs[12:12 + len(ex_in)] + refs[20 + len(ex_in):20 + len(ex_in) + len(ex_out_shape)] + refs[-2:]
        heads = range(SCAN_HEADS_BWD)
        chunks = NCH // SCAN_PARTS_BWD

        if exchange is not None:
            @pl.when((pl.program_id(0) == 0) & (pl.program_id(1) == 0))
            def _():
                exchange.start(*exchange.split(ex_refs))

        @pl.when((pl.program_id(0) == 0) & (pl.program_id(1) == 0))
        def _():
            dwn_ref[...] = jnp.zeros_like(dwn_ref)

        @pl.when(pl.program_id(1) == 0)
        def _():
            carry_ref[...] = jnp.zeros_like(carry_ref)

        wn = wn_ref[...]
        for hh in heads:
            c = _head_cols(hh)
            ov = o_ref[:, c]
            zv = z_ref[:, c]
            g = dy_ref[:, c]
            sig = _sigmoid(zv)
            dz_ref[:, c] = (g * (ov * _rms_scale(ov) * wn) * sig * (1.0 + zv * (1.0 - sig))).astype(BF16)
            do, dwt = _rms_bwd(ov, wn, g * zv * sig)
            do_ref[:, c] = do
            dwn_ref[...] += jnp.sum(dwt, axis=0, keepdims=True)

        def step(t, dstates):
            ci = chunks - 1 - t
            rows = pl.ds(pl.multiple_of(ci * CHUNK, CHUNK), CHUNK)
            cols = [_head_cols(hh) for hh in heads]
            state = [st_ref[hh, ci] for hh in heads]
            dov = [do_ref[rows, cols[hh]] for hh in heads]
            wv = [w_ref[rows, cols[hh]] for hh in heads]
            ws = [_dot(wv[hh], state[hh]) for hh in heads]
            adov = [_dot(a_ref[hh, rows, :], dov[hh], 0, 0) for hh in heads]
            kds = [_dot(kd_ref[rows, cols[hh]], dstates[hh]) for hh in heads]
            dqd = [_dot(dov[hh], state[hh], 1, 1) for hh in heads]
            qdo = [_dot(qd_ref[rows, cols[hh]], dov[hh], 0, 0) for hh in heads]
            vn = [u_ref[rows, cols[hh]] - ws[hh] for hh in heads]
            dvn = [adov[hh] + kds[hh] for hh in heads]
            da = [_dot(dov[hh], vn[hh], 1, 1) for hh in heads]
            dkd = [_dot(vn[hh], dstates[hh], 1, 1) for hh in heads]
            dwv = [_dot(dvn[hh], state[hh], 1, 1) for hh in heads]
            wdv = [_dot(wv[hh], dvn[hh], 0, 0) for hh in heads]
            for hh in heads:
                da_ref[hh, rows, :] = da[hh]
                dqd_ref[rows, cols[hh]] = dqd[hh]
                dkd_ref[rows, cols[hh]] = dkd[hh]
                dgl = jnp.sum(jnp.sum(dstates[hh] * state[hh], axis=1, keepdims=True), axis=0, keepdims=True)
                dgl_ref[hh, ci] = jnp.broadcast_to(dgl, (1, LANES))
                du_ref[rows, cols[hh]] = dvn[hh]
                dw_ref[rows, cols[hh]] = -dwv[hh]
            return tuple(dstates[hh] * gl_ref[hh, ci] + qdo[hh] - wdv[hh] for hh in heads)

        last = lax.fori_loop(0, chunks, step, tuple(carry_ref[hh] for hh in heads))
        for hh in heads:
            carry_ref[hh] = last[hh]

        if exchange is not None:
            @pl.when((pl.program_id(0) == groups - 1) & (pl.program_id(1) == SCAN_PARTS_BWD - 1))
            def _():
                exchange.finish(*exchange.split(ex_refs))

    hs, a_s, gl_s, st_s, gz_s, mix_s = _scan_specs(SCAN_HEADS_BWD, SCAN_PARTS_BWD, True)
    vec = pl.BlockSpec((1, GHD), lambda g, p: (0, 0))
    tok = jax.ShapeDtypeStruct((S, DGDN), F32)
    res = pl.pallas_call(
        body, name="gdn_scan_bwd", grid=(groups, SCAN_PARTS_BWD),
        in_specs=[mix_s, hs, gz_s, vec, hs, hs, hs, hs, a_s, gl_s, st_s, pl.BlockSpec(memory_space=pl.ANY)] + ex_in_specs,
        out_specs=[gz_s, hs, hs, hs, hs, a_s, gl_s, vec] + ex_out_specs,
        out_shape=[jax.ShapeDtypeStruct((S, DPROJ_PAD), BF16), tok, tok, tok, tok,
                   jax.ShapeDtypeStruct((NGH, S, CHUNK), F32), jax.ShapeDtypeStruct((NGH, NCH, 1, LANES), F32),
                   jax.ShapeDtypeStruct((1, GHD), F32)] + ex_out_shape,
        input_output_aliases={11: 0},
        scratch_shapes=[pltpu.VMEM((S // SCAN_PARTS_BWD, SCAN_HEADS_BWD * GHD), F32),
                        pltpu.VMEM((SCAN_HEADS_BWD, GHD, GHD), F32)] + ex_scratch,
        compiler_params=_cparams(),
    )(dmix, o, proj, w_norm, u, w, qd, kd, a, gl, states, dproj, *ex_in)
    return res[:8], res[8:]


def _place():
    return lax.axis_index("x"), lax.axis_index("y"), lax.axis_index("c")


def _other_chips(x, y):
    return [(1 - x, y), (x, 1 - y), (1 - x, 1 - y)]


HBM = pl.BlockSpec(memory_space=pltpu.HBM)
VMEM = pl.BlockSpec(memory_space=pltpu.VMEM)


def _half_rows(ref_or_rows, half):
    rows = ref_or_rows // 2
    return pl.ds(pl.multiple_of(half * rows, rows), rows)


class _Exchange:
    def __init__(self, inputs, out_shape, n_sems, start, finish=None, middle=None, rest=None):
        self.inputs, self.out_shape, self.n_sems, self.start = inputs, out_shape, n_sems, start
        if finish is None:
            def finish(*refs):
                middle(*refs)
                rest(*refs)
        self.finish = finish
        self.middle = middle if middle is not None else (lambda *refs: None)
        self.rest = rest if rest is not None else finish

    def sem_shapes(self):
        return [pltpu.SemaphoreType.DMA((self.n_sems,)), pltpu.SemaphoreType.DMA((self.n_sems,))]

    def split(self, refs):
        n_in, n_out = len(self.inputs), len(self.out_shape)
        return refs[:n_in], refs[n_in:n_in + n_out], refs[n_in + n_out], refs[n_in + n_out + 1]


def _allgather_exchange(shards, whole=()):
    n, nw = len(shards), len(whole)
    slots = 8

    def plan(src, outs, send_sems, recv_sems):
        x, y, c = _place()
        via_x, via_y, diagonal = _other_chips(x, y)
        id_x, id_y, id_diagonal = [2 * chip[0] + chip[1] for chip in (via_x, via_y, diagonal)]
        me, sibling = (x, y, c), (x, y, 1 - c)

        def rows_of(a, half, quarter):
            total = src[a].shape[0]
            if quarter is None:
                return _half_rows(total, half)
            return pl.ds(pl.multiple_of(half * (total // 2) + quarter * (total // 4), total // 4), total // 4)

        def copy(a, k, chip_index, half, quarter, to, from_src=False):
            rows = rows_of(a, half, quarter)
            dst = outs[a].at[chip_index, rows]
            return pltpu.make_async_remote_copy(
                src_ref=src[a].at[rows] if from_src else dst, dst_ref=dst, send_sem=send_sems.at[slots * a + k],
                recv_sem=recv_sems.at[slots * a + k], device_id=to, device_id_type=MESH)

        def whole_copy(b, k, chip_index, to):
            return pltpu.make_async_remote_copy(
                src_ref=src[n + b], dst_ref=outs[n + b].at[chip_index], send_sem=send_sems.at[slots * n + 3 * b + k],
                recv_sem=recv_sems.at[slots * n + 3 * b + k], device_id=to, device_id_type=MESH)

        first, stages, last = [], [], []
        for a in range(n):
            first += [copy(a, 0, 2 * x + y, c, None, (*via_x, c), True), copy(a, 1, 2 * x + y, c, None, (*via_y, c), True)]
            stages.append([
                (copy(a, 0, id_x, c, None, me),
                 [copy(a, 2, id_x, c, 0, (*via_y, c)), copy(a, 4, id_x, c, None, sibling)]),
                (copy(a, 1, id_y, c, None, me),
                 [copy(a, 3, id_y, c, 1, (*via_x, c)), copy(a, 5, id_y, c, None, sibling)]),
                (copy(a, 2, id_diagonal, c, 0, me), [copy(a, 6, id_diagonal, c, 0, sibling)]),
                (copy(a, 3, id_diagonal, c, 1, me), [copy(a, 7, id_diagonal, c, 1, sibling)]),
            ])
            last += [copy(a, 4, id_x, 1 - c, None, me), copy(a, 5, id_y, 1 - c, None, me),
                     copy(a, 6, id_diagonal, 1 - c, 0, me), copy(a, 7, id_diagonal, 1 - c, 1, me)]
        for b in range(nw):
            for k, (chip, index) in enumerate(((via_x, id_x), (via_y, id_y), (diagonal, id_diagonal))):
                first.append(whole_copy(b, k, 2 * x + y, (*chip, c)))
                last.append(whole_copy(b, k, index, me))
        return first, stages, last

    def start(*refs):
        for cp in plan(*refs)[0]:
            cp.start()

    def pass_on(stages, which):
        for stage in which:
            for per_shard in stages:
                lands, onward = per_shard[stage]
                lands.wait_recv()
                for cp in onward:
                    cp.start()

    def middle(*refs):
        pass_on(plan(*refs)[1], (0, 1))

    def rest(*refs):
        first, stages, last = plan(*refs)
        pass_on(stages, (2, 3))
        for cp in last:
            cp.wait_recv()
        for cp in first + [cp for per_shard in stages for _, onward in per_shard for cp in onward]:
            cp.wait_send()

    out_shape = [jax.ShapeDtypeStruct((NCHIP,) + s.shape, s.dtype) for s in list(shards) + list(whole)]
    return _Exchange(list(shards) + list(whole), out_shape, slots * n + 3 * nw, start, middle=middle, rest=rest)


def _with_own(gathered, own):
    x, y, _ = _place()
    return lax.dynamic_update_index_in_dim(gathered, own, 2 * x + y, axis=0)


def _simple_exchange(inputs, out_shape, copies_of):
    def start(*refs):
        for cp in copies_of(*refs):
            cp.start()

    def finish(*refs):
        for cp in copies_of(*refs):
            cp.wait()

    return _Exchange(list(inputs), out_shape, len(out_shape) * 3, start, finish)


def _pair_exchange(grads):
    def copies_of(src, outs, send_sems, recv_sems):
        x, y, c = _place()
        return [pltpu.make_async_remote_copy(
            src_ref=src[a].at[:, _half_rows(src[a].shape[1], 1 - c)], dst_ref=outs[a], send_sem=send_sems.at[a],
            recv_sem=recv_sems.at[a], device_id=(x, y, 1 - c), device_id_type=MESH) for a in range(len(src))]

    return _simple_exchange(
        grads, [jax.ShapeDtypeStruct((g.shape[0], g.shape[1] // 2, g.shape[2]), g.dtype) for g in grads], copies_of)


def _pair_sum(grads, theirs, name):
    n = len(grads)

    def body(*refs):
        south = lax.axis_index("c") == 0
        for a in range(n):
            g = refs[a][...]
            half = g.shape[0] // 2
            mine = jnp.where(south, g[:half], g[half:])
            refs[2 * n + a][...] = (mine.astype(F32) + refs[n + a][...].astype(F32)).astype(BF16)

    def specs(arrs):
        return [pl.BlockSpec((None,) + g.shape[1:], lambda j: (j, 0, 0)) for g in arrs]

    return pl.pallas_call(
        body, name=name, grid=(NCHIP,), in_specs=specs(grads) + specs(theirs), out_specs=specs(theirs),
        out_shape=[jax.ShapeDtypeStruct(g.shape, BF16) for g in theirs], compiler_params=_cparams(),
    )(*grads, *theirs)


def _chip_exchange(parts):
    def copies_of(src, outs, send_sems, recv_sems):
        x, y, c = _place()
        return [pltpu.make_async_remote_copy(
            src_ref=src[a].at[2 * chip[0] + chip[1]], dst_ref=outs[a].at[k], send_sem=send_sems.at[3 * a + k],
            recv_sem=recv_sems.at[3 * a + k], device_id=(*chip, c), device_id_type=MESH)
            for a in range(len(src)) for k, chip in enumerate(_other_chips(x, y))]

    return _simple_exchange(parts, [jax.ShapeDtypeStruct((NCHIP - 1,) + p.shape[1:], p.dtype) for p in parts], copies_of)


def _chip_sum(parts, received, exchange=None):
    n = len(parts)
    steps = 4
    ex_in, ex_in_specs, ex_out_specs, ex_out_shape, ex_scratch = _hosted(exchange)
    n_ex = len(ex_in)

    def body(*refs):
        mine, theirs = refs[2 * n + n_ex:3 * n + n_ex], refs[3 * n + n_ex:4 * n + n_ex]
        ex_refs = refs[2 * n:2 * n + n_ex] + refs[4 * n + n_ex:len(refs) - n - 2]
        tiles, send_sems, recv_sems = refs[len(refs) - n - 2:len(refs) - 2], refs[-2], refs[-1]
        step = pl.program_id(0)
        if exchange is not None:
            @pl.when(step == 0)
            def _():
                exchange.start(*exchange.split(ex_refs))

        x, y, c = _place()

        def share(a, i):
            rows = tiles[a].shape[1]
            return pltpu.make_async_remote_copy(
                src_ref=tiles[a].at[i], dst_ref=theirs[a].at[pl.ds(pl.multiple_of(i * rows, rows), rows)],
                send_sem=send_sems.at[a * steps + i], recv_sem=recv_sems.at[a * steps + i], device_id=(x, y, 1 - c),
                device_id_type=MESH)

        chip = 2 * x + y
        for a in range(n):
            p, r = refs[a], refs[n + a]
            own = jnp.where(chip == 0, p[0], jnp.where(chip == 1, p[1], jnp.where(chip == 2, p[2], p[3])))
            total = ((own.astype(F32) + r[0].astype(F32)) + r[1].astype(F32)) + r[2].astype(F32)
            mine[a][...] = total
            tiles[a][step] = total
            share(a, step).start()

        @pl.when(step == steps - 1)
        def _():
            if exchange is not None:
                exchange.finish(*exchange.split(ex_refs))
            for a in range(n):
                for i in range(steps):
                    share(a, i).wait()

    def specs(arrs):
        return [pl.BlockSpec((g.shape[0], g.shape[1] // steps, g.shape[2]), lambda i: (0, i, 0)) for g in arrs]

    out_specs = [pl.BlockSpec((g.shape[1] // steps, g.shape[2]), lambda i: (i, 0)) for g in parts]
    halves = [jax.ShapeDtypeStruct(g.shape[1:], F32) for g in parts]
    res = pl.pallas_call(
        body, name="grads_chip_sum", grid=(steps,), in_specs=specs(parts) + specs(received) + ex_in_specs,
        out_specs=out_specs + [HBM] * n + ex_out_specs, out_shape=halves * 2 + ex_out_shape,
        scratch_shapes=ex_scratch + [pltpu.VMEM((steps, g.shape[1] // steps, g.shape[2]), F32) for g in parts]
        + [pltpu.SemaphoreType.DMA((n * steps,))] * 2, compiler_params=_cparams(),
    )(*parts, *received, *ex_in)
    return res[:n], res[n:2 * n], res[2 * n:]


def _adamw_math(w, g, m, v):
    nm = ADAM_B1 * m + (1.0 - ADAM_B1) * g
    nv = ADAM_B2 * v + (1.0 - ADAM_B2) * jnp.square(g)
    m_hat = nm / (1.0 - ADAM_B1 ** ADAM_STEP)
    v_hat = nv / (1.0 - ADAM_B2 ** ADAM_STEP)
    return -ADAM_LR * (m_hat / (jnp.sqrt(v_hat) + ADAM_EPS) + ADAM_WD * w), nm, nv


def _adamw_big(ws, g_mine, g_theirs, ms, vs, exchange=None):
    n = len(ws)
    steps = 8
    ex_in, ex_in_specs, ex_out_specs, ex_out_shape, ex_scratch = _hosted(exchange)

    def body(*refs):
        ex_refs = refs[5 * n:5 * n + len(ex_in)] + refs[9 * n + len(ex_in):]
        outs = refs[5 * n + len(ex_in):9 * n + len(ex_in)]
        if exchange is not None:
            @pl.when(pl.program_id(0) == 0)
            def _():
                exchange.start(*exchange.split(ex_refs))

        own_half = (pl.program_id(0) // (steps // 2)) == lax.axis_index("c")
        for a in range(n):
            g = jnp.where(own_half, refs[n + a][...], refs[2 * n + a][...])
            d, nm, nv = _adamw_math(refs[a][...], g, refs[3 * n + a][...], refs[4 * n + a][...])
            outs[a][...] = g
            outs[n + a][...] = d
            outs[2 * n + a][...] = nm
            outs[3 * n + a][...] = nv

        if exchange is not None:
            @pl.when(pl.program_id(0) == steps - 1)
            def _():
                exchange.finish(*exchange.split(ex_refs))

    specs = [pl.BlockSpec((w.shape[0] // steps, w.shape[1]), lambda i: (i, 0)) for w in ws]
    half_specs = [pl.BlockSpec((g.shape[0] // (steps // 2), g.shape[1]), lambda i: (i % (steps // 2), 0)) for g in g_mine]
    shapes = [jax.ShapeDtypeStruct(w.shape, F32) for w in ws]
    res = pl.pallas_call(
        body, name="adamw_big", grid=(steps,), in_specs=specs + half_specs * 2 + specs * 2 + ex_in_specs,
        out_specs=specs * 4 + ex_out_specs, out_shape=shapes * 4 + ex_out_shape, scratch_shapes=ex_scratch,
        compiler_params=_cparams(),
    )(*ws, *g_mine, *g_theirs, *ms, *vs, *ex_in)
    return res[:n], res[n:2 * n], res[2 * n:3 * n], res[3 * n:4 * n], res[4 * n:]


def _adamw_in(w, g_mine, g_theirs, m, v):
    half = D // 2

    def body(w_ref, gm_ref, gt_ref, m_ref, v_ref, g_out, d_out, nm_out, nv_out, g_ref):
        south = lax.axis_index("c") == 0
        g_ref[0:half, :] = jnp.where(south, gm_ref[...], gt_ref[...])
        g_ref[half:D, :] = jnp.where(south, gt_ref[...], gm_ref[...])
        g = g_ref[0:CW, :]
        d, nm, nv = _adamw_math(w_ref[...], g, m_ref[...], v_ref[...])
        g_out[...] = g
        d_out[...] = d
        nm_out[...] = nm
        nv_out[...] = nv

    spec = pl.BlockSpec((CW, LANES), lambda i: (0, i))
    half_spec = pl.BlockSpec((half, LANES), lambda i: (0, i))
    return pl.pallas_call(
        body, name="adamw_in", grid=(D // LANES,), in_specs=[spec, half_spec, half_spec, spec, spec], out_specs=[spec] * 4,
        out_shape=[jax.ShapeDtypeStruct((CW, D), F32)] * 4, scratch_shapes=[pltpu.VMEM((D, LANES), F32)],
        compiler_params=_cparams(),
    )(w, g_mine, g_theirs, m, v)


NORM_NAMES = ("pre_mix_norm", "post_mix_norm", "pre_mlp_norm", "post_mlp_norm")
SMALL_NAMES = NORM_NAMES + ("gdn_conv_w", "fox_f_bias", "gdn_dt_bias", "gdn_a_log", "fox_out_norm", "gdn_out_norm")
CONV_COLS = 3 * DGDN // NCHIP


def _small_gather(d_norms, d_conv, sums, d_fox_norm, d_gdn_norm, loss_row):
    n_arrays = 6
    n_remote = n_arrays * (NDEV - 1)

    def copies_of(src, outs, send_sems, recv_sems):
        x, y, c = _place()
        me = 4 * x + 2 * y + c

        def from_me(chip_index):
            cols = pl.ds(pl.multiple_of(chip_index * CONV_COLS, LANES), CONV_COLS)
            return [src[0], src[1].at[:, cols], src[2], src[3], src[4], src[5]]

        local = [pltpu.make_async_copy(s, outs[a].at[me], send_sems.at[n_remote + a]) for a, s in enumerate(from_me(2 * x + y))]
        remote = []
        for k in range(1, NDEV):
            px, py, pc = x ^ ((k >> 2) & 1), y ^ ((k >> 1) & 1), c ^ (k & 1)
            remote += [pltpu.make_async_remote_copy(
                src_ref=s, dst_ref=outs[a].at[me], send_sem=send_sems.at[n_arrays * (k - 1) + a],
                recv_sem=recv_sems.at[n_arrays * (k - 1) + a], device_id=(px, py, pc), device_id_type=MESH)
                for a, s in enumerate(from_me(2 * px + py))]
        return local + remote

    def start(*refs):
        for cp in copies_of(*refs):
            cp.start()

    def finish(*refs):
        for cp in copies_of(*refs):
            cp.wait()

    shapes = [(4, D), (CONV_K, CONV_COLS), (8, LANES), (1, LANES), (1, LANES), (1, LANES)]
    return _Exchange([d_norms, d_conv, sums, d_fox_norm, d_gdn_norm, loss_row],
                     [jax.ShapeDtypeStruct((NDEV,) + s, F32) for s in shapes], n_remote + n_arrays, start, finish)


def _small_adamw(gathered, ws, ms, vs):
    n = len(SMALL_NAMES)
    ng = len(gathered)

    def body(*refs):
        def total(buf):
            acc = buf[0]
            for i in range(1, NDEV):
                acc = acc + buf[i]
            return acc

        t_norms, t_conv, t_sums, t_fn, t_gn, t_loss = [total(r) for r in refs[:ng]]
        w_refs, m_refs, v_refs = refs[ng:ng + n], refs[ng + n:ng + 2 * n], refs[ng + 2 * n:ng + 3 * n]
        outs = refs[ng + 3 * n:]
        outs[4 * n][...] = t_loss
        grads = [t_norms[i:i + 1, :] for i in range(4)] + [
            t_conv, t_sums[0:1, 0:NFH], t_sums[1:2, 0:NGH], t_sums[2:3, 0:NGH], t_fn[:, 0:FHD], t_gn]
        for a in range(n):
            d, nm, nv = _adamw_math(w_refs[a][...], grads[a], m_refs[a][...], v_refs[a][...])
            outs[a][...] = grads[a]
            outs[n + a][...] = d
            outs[2 * n + a][...] = nm
            outs[3 * n + a][...] = nv

    def whole(arr):
        return pl.BlockSpec(arr.shape, lambda i: (0,) * arr.ndim)

    res = pl.pallas_call(
        body, name="small_adamw", grid=(1,), in_specs=[whole(t) for t in gathered] + [whole(w) for w in ws] * 3,
        out_specs=[whole(w) for w in ws] * 4 + [pl.BlockSpec((1, LANES), lambda i: (0, 0))],
        out_shape=[jax.ShapeDtypeStruct(w.shape, F32) for w in ws] * 4 + [jax.ShapeDtypeStruct((1, LANES), F32)],
        compiler_params=_cparams(),
    )(*gathered, *ws, *ms, *vs)
    return res[:n], res[n:2 * n], res[2 * n:3 * n], res[3 * n:4 * n], res[4 * n]


CW = DPROJ // NCHIP
PROJ_RUNS = tuple((part * DFOX + hp * LANES, part * DFOX + (hp + 1) * LANES, (3 * hp + part) * LANES)
                  for hp in range(NPAIR) for part in range(3)) + (
    (1536, 1544, BLK_SMALL * LANES), (1544, 3080, BLK_GDN * LANES), (3080, 3088, BLK_SMALL * LANES + 8),
    (3088, 3600, BLK_GZ * LANES))


def _proj_pieces():
    pieces = []
    for lo, hi, at in PROJ_RUNS:
        while lo < hi:
            j = lo // CW
            end = min(hi, (j + 1) * CW)
            pieces.append((j, lo - j * CW, at, end - lo))
            at, lo = at + end - lo, end
    return pieces


RT = 256


def _to_padded_rows(gathered):
    def body(src_ref, out_ref, blocks_ref, rows_ref):
        blocks_ref[...] = src_ref[...].astype(F32)
        rows_ref[...] = jnp.zeros_like(rows_ref)
        for j, start, at, n in _proj_pieces():
            rows_ref[at:at + n, :] = blocks_ref[j, start:start + n, :]
        out_ref[...] = rows_ref[...].astype(out_ref.dtype)

    return pl.pallas_call(
        body, name="proj_rows_in", grid=(D // RT,), in_specs=[pl.BlockSpec((NCHIP, D, RT), lambda i: (0, 0, i))],
        out_specs=pl.BlockSpec((DPROJ_PAD, RT), lambda i: (0, i)), out_shape=jax.ShapeDtypeStruct((DPROJ_PAD, D), gathered.dtype),
        scratch_shapes=[pltpu.VMEM((NCHIP, D, RT), F32), pltpu.VMEM((DPROJ_PAD, RT), F32)], compiler_params=_cparams(),
    )(gathered)


def _from_padded_rows_pair_sum(w):
    steps = D // RT
    half = D // 2

    def body(src_ref, out_ref, rows_ref, blocks_ref, mine_ref, send_ref, recv_ref, send_sems, recv_sems):
        i = pl.program_id(0)
        x, y, c = _place()

        def share(t):
            return pltpu.make_async_remote_copy(
                src_ref=send_ref.at[t], dst_ref=recv_ref.at[t], send_sem=send_sems.at[t], recv_sem=recv_sems.at[t],
                device_id=(x, y, 1 - c), device_id_type=MESH)

        def rows_of(core):
            return blocks_ref[:, pl.ds(pl.multiple_of(core * half, half), half), :]

        @pl.when(i < steps)
        def _():
            rows_ref[...] = src_ref[...].astype(F32)
            blocks_ref[...] = jnp.zeros_like(blocks_ref)
            for j, start, at, n in _proj_pieces():
                blocks_ref[j, start:start + n, :] = rows_ref[at:at + n, :]
            mine_ref[i % 2] = rows_of(c)
            send_ref[i] = rows_of(1 - c).astype(BF16)
            share(i).start()

        @pl.when(i > 0)
        def _():
            share(i - 1).wait_recv()
            out_ref[...] = (mine_ref[(i - 1) % 2] + recv_ref[i - 1].astype(F32)).astype(BF16)

        @pl.when(i == steps)
        def _():
            for t in range(steps):
                share(t).wait_send()

    tile = (NCHIP, half, RT)
    return pl.pallas_call(
        body, name="proj_rows_out_pair_sum", grid=(steps + 1,),
        in_specs=[pl.BlockSpec((DPROJ_PAD, RT), lambda i: (0, jnp.minimum(i, steps - 1)))],
        out_specs=pl.BlockSpec(tile, lambda i: (0, 0, jnp.maximum(i - 1, 0))),
        out_shape=jax.ShapeDtypeStruct((NCHIP, half, D), BF16),
        scratch_shapes=[pltpu.VMEM((DPROJ_PAD, RT), F32), pltpu.VMEM((NCHIP, D, RT), F32), pltpu.VMEM((2,) + tile, F32),
                        pltpu.VMEM((steps,) + tile, BF16), pltpu.VMEM((steps,) + tile, BF16),
                        pltpu.SemaphoreType.DMA((steps,)), pltpu.SemaphoreType.DMA((steps,))],
        compiler_params=_cparams(),
    )(w)


def _local_step(x, target, first_weights, late_weights, reduce_late, reduce_in, pre_mix_norm, fox_f_bias, fox_out_norm,
                gdn_a_log, gdn_dt_bias, gdn_out_norm, post_mix_norm, pre_mlp_norm, post_mlp_norm):
    bias_vec = jnp.zeros((1, LANES), F32).at[0, 0:NFH].set(fox_f_bias).at[0, LANE_G:LANE_G + NGH].set(gdn_dt_bias)
    alog_vec = jnp.zeros((1, LANES), F32).at[0, LANE_G:LANE_G + NGH].set(gdn_a_log)
    w2 = jnp.concatenate([fox_out_norm, fox_out_norm], axis=1)

    h, first = _pre_norm(x, pre_mix_norm, exchange=first_weights[0])
    win_p, conv_w = first_weights[1](first)
    proj = _matmul(h, win_p, tb=True, tm=2048, tn=768, tk=1024, name="mm_proj", exchange=late_weights[0])
    proj, late_a = proj if late_weights[0] is not None else (proj, [])
    gates = _gates(proj, bias_vec, alog_vec)
    mix, fox_o, lse, late_b = _fox_fwd(proj, gates, w2, exchange=late_weights[1])
    qkv = _gdn_pre(proj, conv_w)
    (u, w, qd, kd, a_intra, gl, t_inv), _ = _gdn_prep(qkv, gates)
    wout, wup3 = late_weights[3](late_a, late_b)
    mix, gdn_raw, states = _gdn_scan(u, w, qd, kd, a_intra, gl, proj, gdn_out_norm, mix)
    def post_mix(acc, xv, w_post, w_pre_mlp):
        x1v = xv + acc * _rms_scale(acc) * w_post
        return acc, x1v, x1v * _rms_scale(x1v) * w_pre_mlp

    mixed, x1, h2 = _matmul(mix, wout, tm=512, tn=D, tk=1024, out_dtypes=(F32, F32, BF16), name="mm_out",
                            extra=(x, post_mix_norm, pre_mlp_norm), epilogue=post_mix)

    def relu2(acc):
        r = jnp.maximum(acc, 0.0)
        return r, r * r

    up_act = _matmul(h2, wup3, b3=True, tm=1024, tn=1024, tk=1024, out_dtypes=(BF16, BF16), epilogue=relu2,
                     name="mm_up", exchange=late_weights[2])
    (up_relu, act), late_c = up_act if late_weights[2] is not None else (up_act, [])
    wdown = late_weights[4](late_c)
    def loss_head(acc, x1v, tv, w):
        err = x1v + acc * _rms_scale(acc) * w - tv
        dx2v = err * (1.0 / D)
        dyv, dwt = _rms_bwd(acc, w, dx2v)
        part = 0.5 * jnp.sum(jnp.mean(err * err, axis=-1, keepdims=True), axis=0, keepdims=True)
        return dx2v, dyv, jnp.sum(dwt, axis=0, keepdims=True), jnp.broadcast_to(part, (1, D))

    dx2, dy, d_post_mlp, loss_wide = _matmul(
        act, wdown, tm=512, tn=D, tk=DFF, out_dtypes=(F32, BF16, F32, F32), extra=(x1, target, post_mlp_norm),
        epilogue=loss_head, n_sums=2, name="mm_down")
    loss_row = loss_wide[:, :LANES]

    dwdown = _matmul(act, dy, ta=True, tm=1024, tn=1024, tk=2048, out_dtypes=(BF16,), name="mm_dwdown")

    def relu2_bwd(acc, r):
        return (acc * 2.0 * r.astype(F32),)

    dup = _matmul(dy, wdown, tb=True, tm=1024, tn=1024, tk=1024, out_dtypes=(BF16,), extra=(up_relu,), epilogue=relu2_bwd,
                  name="mm_dact")
    dwup3 = _matmul(h2, dup, ta=True, tm=1024, tn=1024, tk=2048, out_dtypes=(BF16,), o3=True, name="mm_dwup")
    def mid_bwd(acc, x1v, dx2v, mixedv, w_pre_mlp, w_post):
        dxa, dwm = _rms_bwd(x1v, w_pre_mlp, acc)
        dx1v = dx2v + dxa
        dm, dwp = _rms_bwd(mixedv, w_post, dx1v)
        return dx1v, dm, jnp.sum(dwm, axis=0, keepdims=True), jnp.sum(dwp, axis=0, keepdims=True)

    dx1, dmixed, d_pre_mlp, d_post_mix = _matmul(
        dup, wup3, tb=True, b3=True, tm=512, tn=D, tk=DFF, out_dtypes=(F32, BF16, F32, F32),
        extra=(x1, dx2, mixed, pre_mlp_norm, post_mix_norm), epilogue=mid_bwd, n_sums=2, name="mm_dh2")
    dwout = _matmul(mix, dmixed, ta=True, tm=1024, tn=1024, tk=2048, out_dtypes=(BF16,), name="mm_dwout")
    dmix = _matmul(dmixed, wout, tb=True, tm=2048, tk=1024, name="mm_dmix")

    dfox, delta, d_fox_norm, from_sibling = _fox_norm_bwd(fox_o, dmix, w2, exchange=reduce_late[0](dwout, dwup3, dwdown))
    dproj, dcum_fox, reduced_a = _fox_bwd(proj, dfox, gates, lse, delta, exchange=reduce_late[1](from_sibling))
    (dproj, du, dw, dqd, dkd, da, dgl, d_gdn_norm), reduced_b = _gdn_scan_bwd(
        dmix, gdn_raw, proj, gdn_out_norm, u, w, qd, kd, a_intra, gl, states, dproj, exchange=reduce_late[2]())
    dqkv, dgates_gdn, reduced_c = _gdn_prep_bwd(qkv, gates, t_inv, du, dw, dqd, dkd, da, dgl, exchange=reduce_late[3]())
    reduced_late = (reduced_a, reduced_b, reduced_c)
    dproj, d_conv = _gdn_pre_bwd(proj, conv_w, dqkv, dproj)
    dproj, sums = _gates_bwd(proj, bias_vec, alog_vec, dgates_gdn, dcum_fox, dproj)

    dwin_p = _matmul(dproj, h, ta=True, tm=1280, tn=1024, tk=2048, out_dtypes=(BF16,), name="mm_dwin")
    exchange_in = reduce_in(dwin_p)
    dh = _matmul(dproj, win_p, tm=1024, tk=DPROJ_PAD, name="mm_dh", exchange=exchange_in)
    dh, reduced_in = dh if exchange_in is not None else (dh, [])
    grad_x, d_pre_mix = _pre_norm_bwd(dh, x, pre_mix_norm, dx1)

    d_norms = jnp.concatenate([d_pre_mix, d_post_mix, d_pre_mlp, d_post_mlp], axis=0)
    return grad_x, (d_norms, d_conv, sums, d_fox_norm, d_gdn_norm, loss_row), reduced_late, reduced_in


def kernel(x, pre_mix_norm, w_in, fox_f_bias, fox_out_norm, gdn_conv_w, gdn_a_log, gdn_dt_bias, gdn_out_norm, w_out, post_mix_norm, pre_mlp_norm, w_up, w_down, post_mlp_norm, loss_target, m_pre_mix_norm, m_w_in, m_fox_f_bias, m_fox_out_norm, m_gdn_conv_w, m_gdn_a_log, m_gdn_dt_bias, m_gdn_out_norm, m_w_out, m_post_mix_norm, m_pre_mlp_norm, m_w_up, m_w_down, m_post_mlp_norm, v_pre_mix_norm, v_w_in, v_fox_f_bias, v_fox_out_norm, v_gdn_conv_w, v_gdn_a_log, v_gdn_dt_bias, v_gdn_out_norm, v_w_out, v_post_mix_norm, v_pre_mlp_norm, v_w_up, v_w_down, v_post_mlp_norm):
    weights = dict(pre_mix_norm=pre_mix_norm, w_in=w_in, fox_f_bias=fox_f_bias, fox_out_norm=fox_out_norm, gdn_conv_w=gdn_conv_w,
                   gdn_a_log=gdn_a_log, gdn_dt_bias=gdn_dt_bias, gdn_out_norm=gdn_out_norm, w_out=w_out, post_mix_norm=post_mix_norm,
                   pre_mlp_norm=pre_mlp_norm, w_up=w_up, w_down=w_down, post_mlp_norm=post_mlp_norm)
    m_in = dict(pre_mix_norm=m_pre_mix_norm, w_in=m_w_in, fox_f_bias=m_fox_f_bias, fox_out_norm=m_fox_out_norm, gdn_conv_w=m_gdn_conv_w,
                gdn_a_log=m_gdn_a_log, gdn_dt_bias=m_gdn_dt_bias, gdn_out_norm=m_gdn_out_norm, w_out=m_w_out, post_mix_norm=m_post_mix_norm,
                pre_mlp_norm=m_pre_mlp_norm, w_up=m_w_up, w_down=m_w_down, post_mlp_norm=m_post_mlp_norm)
    v_in = dict(pre_mix_norm=v_pre_mix_norm, w_in=v_w_in, fox_f_bias=v_fox_f_bias, fox_out_norm=v_fox_out_norm, gdn_conv_w=v_gdn_conv_w,
                gdn_a_log=v_gdn_a_log, gdn_dt_bias=v_gdn_dt_bias, gdn_out_norm=v_gdn_out_norm, w_out=v_w_out, post_mix_norm=v_post_mix_norm,
                pre_mlp_norm=v_pre_mlp_norm, w_up=v_w_up, w_down=v_w_down, post_mlp_norm=v_post_mlp_norm)
    order_w = ("pre_mix_norm", "w_in", "fox_f_bias", "fox_out_norm", "gdn_conv_w", "gdn_a_log", "gdn_dt_bias", "gdn_out_norm", "w_out",
               "post_mix_norm", "pre_mlp_norm", "w_up", "w_down", "post_mlp_norm")
    big = ("w_in", "w_out", "w_up", "w_down")

    def row(v):
        return v if v.ndim == 2 else v.reshape(1, -1)

    win_shard = jnp.pad(w_in.T.astype(BF16), ((0, D - CW), (0, 0)))

    def resolve_first(gathered):
        win_g, conv_g = gathered
        return (_to_padded_rows(_with_own(win_g, win_shard)),
                _with_own(conv_g, gdn_conv_w).transpose(1, 0, 2).reshape(CONV_K, 3 * DGDN))

    late_shards = [weights[n].astype(BF16) for n in big[1:]]

    gathered_down = []

    def resolve_out_up(gathered_out, gathered_mlp):
        gathered_down.append(gathered_mlp[1])
        return _with_own(gathered_out[0], late_shards[0]).reshape(D, D), _with_own(gathered_mlp[0], late_shards[1])

    def resolve_down(_):
        return _with_own(gathered_down[0], late_shards[2]).reshape(DFF, D)

    pair_sums, late_blocks = {}, []

    def pair_summed(names, blocks, theirs):
        for n, s in zip(names, _pair_sum(blocks, theirs, "grads_pair_sum_" + names[0])):
            pair_sums[n] = s

    def late_pair_exchange(dwout, dwup3, dwdown):
        late_blocks.extend([dwout.reshape(NCHIP, D // NCHIP, D), dwup3, dwdown.reshape(NCHIP, DFF // NCHIP, D)])
        return _pair_exchange(late_blocks)

    def late_chip_exchange(theirs):
        pair_summed(big[1:], late_blocks, theirs)
        return _chip_exchange([pair_sums["w_up"], pair_sums["w_down"]])

    def reduce_in(dwin_p):
        pair_sums["w_in"] = _from_padded_rows_pair_sum(dwin_p)
        return _chip_exchange([pair_sums["w_in"]])

    grad_x, small, received_late, received_in = _local_step(
        x[0], loss_target[0], (_allgather_exchange([win_shard], whole=[gdn_conv_w]), resolve_first),
        (_allgather_exchange(late_shards[:1]), _allgather_exchange(late_shards[1:]), None, resolve_out_up, resolve_down),
        (late_pair_exchange, late_chip_exchange, lambda: None, lambda: _chip_exchange([pair_sums["w_out"]])),
        reduce_in, row(pre_mix_norm), fox_f_bias, row(fox_out_norm), gdn_a_log, gdn_dt_bias,
        row(gdn_out_norm), row(post_mix_norm), row(pre_mlp_norm), row(post_mlp_norm))
    received_mlp, _, received_out = received_late

    g_mine, g_theirs, small_gathered = _chip_sum(
        [pair_sums[n] for n in big], list(received_in[:1]) + list(received_out[:1]) + list(received_mlp[:2]),
        exchange=_small_gather(*small))

    g_big, d_big, nm_big, nv_big, _ = _adamw_big(
        [weights[n] for n in big[1:]], g_mine[1:], g_theirs[1:], [m_in[n] for n in big[1:]], [v_in[n] for n in big[1:]])
    in_t = _adamw_in(w_in.T, g_mine[0], g_theirs[0], m_w_in.T, v_w_in.T)
    g_small, d_small, nm_small, nv_small, loss_total = _small_adamw(
        small_gathered, [row(weights[n]) for n in SMALL_NAMES], [row(m_in[n]) for n in SMALL_NAMES],
        [row(v_in[n]) for n in SMALL_NAMES])

    grads, delta, new_m, new_v = {}, {}, {}, {}
    grads["w_in"], delta["w_in"], new_m["w_in"], new_v["w_in"] = [t.T for t in in_t]
    for i, n in enumerate(big[1:]):
        grads[n], delta[n], new_m[n], new_v[n] = g_big[i], d_big[i], nm_big[i], nv_big[i]
    for i, n in enumerate(SMALL_NAMES):
        shape = weights[n].shape
        grads[n], delta[n], new_m[n], new_v[n] = (g_small[i].reshape(shape), d_small[i].reshape(shape),
                                                  nm_small[i].reshape(shape), nv_small[i].reshape(shape))
    return (loss_total[0, 0], grad_x[None], *[grads[n] for n in order_w], *[delta[n] for n in order_w], *[new_m[n] for n in order_w],
            *[new_v[n] for n in order_w])
```

```python
import jax
import jax.numpy as jnp
from jax import lax
from jax.experimental import pallas as pl
from jax.experimental.pallas import tpu as pltpu

F32 = jnp.float32
BF16 = jnp.bfloat16
MESH = pl.DeviceIdType.MESH

S = 2048
D = 1024
NFH, FHD = 8, 64
NPAIR = NFH // 2
NGH, GHD = 4, 128
DFOX = NFH * FHD
DGDN = NGH * GHD
CHUNK = 64
NCH = S // CHUNK
CONV_K = 4
DFF = 4 * D
EPS = 1e-6
DPROJ = 3600
LANES = 128
DPROJ_PAD = 3840
BLK_GDN = 12
BLK_GZ = 24
BLK_SMALL = 28
NCHIP = 4
NDEV = 8
VMEM_LIMIT = 56 * 1024 * 1024

ADAM_LR = 0.001
ADAM_B1 = 0.9
ADAM_B2 = 0.999
ADAM_EPS = 1e-08
ADAM_WD = 0.01
ADAM_STEP = 10


def _cparams(**kw):
    return pltpu.CompilerParams(vmem_limit_bytes=VMEM_LIMIT, **kw)


def _dn(ca, cb):
    return (((ca,), (cb,)), ((), ()))


def _dot(a, b, ca=1, cb=0):
    return lax.dot_general(a.astype(BF16), b.astype(BF16), _dn(ca, cb), preferred_element_type=F32)


def _hdot(a, b, ca=1, cb=0):
    return lax.dot_general(a.astype(F32), b.astype(F32), _dn(ca, cb), precision=lax.Precision.HIGHEST,
                           preferred_element_type=F32)


def _dot3(a, b, ca=1, cb=0):
    a_hi, b_hi = a.astype(BF16), b.astype(BF16)
    a_lo, b_lo = (a - a_hi.astype(F32)).astype(BF16), (b - b_hi.astype(F32)).astype(BF16)
    dn = _dn(ca, cb)
    return (lax.dot_general(a_hi, b_hi, dn, preferred_element_type=F32)
            + (lax.dot_general(a_hi, b_lo, dn, preferred_element_type=F32)
               + lax.dot_general(a_lo, b_hi, dn, preferred_element_type=F32)))


@jax.custom_vjp
def _mm_nn(a, b):
    return _dot(a, b, 1, 0)


def _mm_nn_fwd(a, b):
    return _dot(a, b, 1, 0), (a, b)


def _mm_nn_bwd(res, g):
    a, b = res
    return _dot(g, b, 1, 1), _dot(a, g, 0, 0)


_mm_nn.defvjp(_mm_nn_fwd, _mm_nn_bwd)


@jax.custom_vjp
def _mm_nt(a, b):
    return _dot(a, b, 1, 1)


def _mm_nt_fwd(a, b):
    return _dot(a, b, 1, 1), (a, b)


def _mm_nt_bwd(res, g):
    a, b = res
    return _dot(g, b, 1, 0), _dot(g, a, 0, 0)


_mm_nt.defvjp(_mm_nt_fwd, _mm_nt_bwd)


@jax.custom_vjp
def _saved_inverse(m, t_inv):
    del m
    return t_inv


def _saved_inverse_fwd(m, t_inv):
    del m
    return t_inv, t_inv


def _saved_inverse_bwd(t_inv, g):
    return -_dot3(_dot3(t_inv, g, 0, 0), t_inv, 1, 1), jnp.zeros_like(t_inv)


_saved_inverse.defvjp(_saved_inverse_fwd, _saved_inverse_bwd)


def _sigmoid(z):
    return 1.0 / (1.0 + jnp.exp(-z))


def _softplus(z):
    return jnp.maximum(z, 0.0) + jnp.log(1.0 + jnp.exp(-jnp.abs(z)))


def _silu(z):
    return z * _sigmoid(z)


def _rms_scale(x):
    return lax.rsqrt(jnp.mean(x * x, axis=-1, keepdims=True) + EPS)


def _rms_bwd(x, w, g):
    r = _rms_scale(x)
    gw = g * w
    dx = r * gw - x * (r * r * r) * jnp.mean(gw * x, axis=-1, keepdims=True)
    return dx, g * x * r


def _matmul(a, b, *, name, ta=False, tb=False, tm=512, tn=512, tk=512, out_dtypes=(F32,), b3=False, o3=False,
            extra=(), epilogue=None, exchange=None, n_sums=0):
    m, k = (a.shape[1], a.shape[0]) if ta else a.shape
    if b3:
        n = b.shape[1] if tb else b.shape[0] * b.shape[2]
        kb = b.shape[0] * b.shape[2] if tb else b.shape[1]
    else:
        n, kb = (b.shape[0], b.shape[1]) if tb else (b.shape[1], b.shape[0])
    assert kb == k, (name, kb, k)
    tm, tn, tk = min(tm, m), min(tn, n), min(tk, k)
    assert m % tm == 0 and n % tn == 0 and k % tk == 0, (name, m, n, k, tm, tn, tk)
    nk = k // tk
    whole_k_blocks = b3 and tb and not ta and nk == 1 and b.shape[0] > 1
    n_extra = len(extra)
    n_out = len(out_dtypes)
    grid = (m // tm, n // tn, nk)
    ex_in, ex_in_specs, ex_out_specs, ex_out_shape, ex_scratch = _hosted(exchange)

    def body(*refs):
        a_ref, b_ref = refs[0], refs[1]
        extra_refs = refs[2:2 + n_extra]
        first_out = 2 + n_extra + len(ex_in)
        out_refs = refs[first_out:first_out + n_out]
        ex_refs = refs[2 + n_extra:first_out] + refs[first_out + n_out:first_out + n_out + len(ex_out_shape)] + refs[-2:]
        step = [pl.program_id(d) for d in range(3)]

        if exchange is not None:
            @pl.when((step[0] == 0) & (step[1] == 0) & (step[2] == 0))
            def _():
                exchange.start(*exchange.split(ex_refs))

        def finish(acc):
            outs = (acc,) if epilogue is None else epilogue(acc, *[r[...] for r in extra_refs])
            for o_ref, val in zip(out_refs[:n_out - n_sums], outs):
                o_ref[...] = val.astype(o_ref.dtype)
            for o_ref, val in zip(out_refs[n_out - n_sums:], outs[n_out - n_sums:]):
                @pl.when(step[0] == 0)
                def _(o_ref=o_ref, val=val):
                    o_ref[...] = val

                @pl.when(step[0] > 0)
                def _(o_ref=o_ref, val=val):
                    o_ref[...] += val

        if whole_k_blocks:
            width = b.shape[2]
            part = _dot(a_ref[:, 0:width], b_ref[0], 1, 1)
            for blk in range(1, b.shape[0]):
                part = part + _dot(a_ref[:, blk * width:(blk + 1) * width], b_ref[blk], 1, 1)
        else:
            part = _dot(a_ref[...], b_ref[...], 0 if ta else 1, 1 if tb else 0)
        if nk == 1:
            finish(part)
        else:
            acc_ref = refs[first_out + n_out + len(ex_out_shape)]

            @pl.when(step[2] == 0)
            def _():
                acc_ref[...] = part

            @pl.when(step[2] > 0)
            def _():
                acc_ref[...] += part

            @pl.when(step[2] == nk - 1)
            def _():
                finish(acc_ref[...])

        if exchange is not None:
            flat = (step[0] * grid[1] + step[1]) * nk + step[2]
            total = grid[0] * grid[1] * nk

            @pl.when(flat == total // 2)
            def _():
                exchange.middle(*exchange.split(ex_refs))

            @pl.when(flat == total - 1)
            def _():
                exchange.rest(*exchange.split(ex_refs))

    a_spec = pl.BlockSpec((tk, tm), lambda i, j, kk: (kk, i)) if ta else pl.BlockSpec((tm, tk), lambda i, j, kk: (i, kk))
    if whole_k_blocks:
        b_spec = pl.BlockSpec((b.shape[0], tn, b.shape[2]), lambda i, j, kk: (0, j, 0))
    elif b3 and tb:
        assert b.shape[2] == tk
        b_spec = pl.BlockSpec((None, tn, tk), lambda i, j, kk: (kk, j, 0))
    elif b3:
        assert b.shape[2] == tn
        b_spec = pl.BlockSpec((None, tk, tn), lambda i, j, kk: (j, kk, 0))
    elif tb:
        b_spec = pl.BlockSpec((tn, tk), lambda i, j, kk: (j, kk))
    else:
        b_spec = pl.BlockSpec((tk, tn), lambda i, j, kk: (kk, j))
    tile = pl.BlockSpec((tm, tn), lambda i, j, kk: (i, j))
    out_specs = [tile] * n_out
    out_shape = [jax.ShapeDtypeStruct((m, n), dt) for dt in out_dtypes]
    if o3:
        out_specs[0] = pl.BlockSpec((None, tm, tn), lambda i, j, kk: (j, i, 0))
        out_shape[0] = jax.ShapeDtypeStruct((n // tn, m, tn), out_dtypes[0])
    assert n_sums == 0 or tn == n
    for r in range(n_out - n_sums, n_out):
        out_specs[r] = pl.BlockSpec((1, tn), lambda i, j, kk: (0, 0))
        out_shape[r] = jax.ShapeDtypeStruct((1, n), out_dtypes[r])
    res = pl.pallas_call(
        body, name=name, grid=grid,
        in_specs=[a_spec, b_spec] + [tile if e.shape[0] == m else pl.BlockSpec((1, tn), lambda i, j, kk: (0, j)) for e in extra]
        + ex_in_specs, out_specs=out_specs + ex_out_specs,
        out_shape=out_shape + ex_out_shape,
        scratch_shapes=([pltpu.VMEM((tm, tn), F32)] if nk > 1 else []) + ex_scratch,
        compiler_params=_cparams(),
    )(a, b, *extra, *ex_in)
    if exchange is not None:
        return (res[0] if n_out == 1 else res[:n_out]), res[n_out:]
    return res[0] if n_out == 1 else res


TR = 256


def _row_spec(cols):
    return pl.BlockSpec((TR, cols), lambda i: (i, 0))


def _vec_spec(cols):
    return pl.BlockSpec((1, cols), lambda i: (0, 0))


def _pre_norm(x, w, exchange=None):
    ex_in, ex_in_specs, ex_out_specs, ex_out_shape, ex_scratch = _hosted(exchange)

    def body(*refs):
        x_ref, w_ref, h_ref = refs[0], refs[1], refs[2 + len(ex_in)]
        ex_refs = refs[2:2 + len(ex_in)] + refs[3 + len(ex_in):]
        if exchange is not None:
            @pl.when(pl.program_id(0) == 0)
            def _():
                exchange.start(*exchange.split(ex_refs))

        xv = x_ref[...]
        h_ref[...] = (xv * _rms_scale(xv) * w_ref[...]).astype(BF16)

        if exchange is not None:
            @pl.when(pl.program_id(0) == S // TR - 1)
            def _():
                exchange.finish(*exchange.split(ex_refs))

    res = pl.pallas_call(
        body, name="pre_norm", grid=(S // TR,), in_specs=[_row_spec(D), _vec_spec(D)] + ex_in_specs,
        out_specs=[_row_spec(D)] + ex_out_specs, out_shape=[jax.ShapeDtypeStruct((S, D), BF16)] + ex_out_shape,
        scratch_shapes=ex_scratch, compiler_params=_cparams(),
    )(x, w, *ex_in)
    return res[0], res[1:]


def _pre_norm_bwd(dh, x, w, dx1):
    def body(dh_ref, x_ref, w_ref, dx1_ref, dx_ref, dw_ref):
        i = pl.program_id(0)
        dxa, dwt = _rms_bwd(x_ref[...], w_ref[...], dh_ref[...])
        dx_ref[...] = dx1_ref[...] + dxa

        @pl.when(i == 0)
        def _():
            dw_ref[...] = jnp.zeros_like(dw_ref)

        dw_ref[...] += jnp.sum(dwt, axis=0, keepdims=True)

    return pl.pallas_call(
        body, name="pre_norm_bwd", grid=(S // TR,),
        in_specs=[_row_spec(D), _row_spec(D), _vec_spec(D), _row_spec(D)], out_specs=[_row_spec(D), _vec_spec(D)],
        out_shape=[jax.ShapeDtypeStruct((S, D), F32), jax.ShapeDtypeStruct((1, D), F32)], compiler_params=_cparams(),
    )(dh, x, w, dx1)


BQ = 512
NQ = S // BQ
LANE_BETA, LANE_G = 8, 12


def _gate_lanes(shape):
    lane = lax.broadcasted_iota(jnp.int32, shape, 1)
    return lane < LANE_BETA, (lane >= LANE_BETA) & (lane < LANE_G), (lane >= LANE_G) & (lane < LANE_G + NGH)


def _gates(proj, bias_vec, alog_vec):
    def body(s_ref, b_ref, a_ref, o_ref, carry_ref):
        i = pl.program_id(0)

        @pl.when(i == 0)
        def _():
            carry_ref[...] = jnp.zeros_like(carry_ref)

        z = s_ref[...] + b_ref[...]
        tail = jnp.log(1.0 + jnp.exp(-jnp.abs(z)))
        sp = jnp.maximum(z, 0.0) + tail
        lf = jnp.minimum(z, 0.0) - tail
        r = lax.broadcasted_iota(jnp.int32, (BQ, BQ), 0)
        c = lax.broadcasted_iota(jnp.int32, (BQ, BQ), 1)
        tri = (c <= r).astype(F32)
        cum = _hdot(tri, lf) + carry_ref[...]
        carry_ref[...] = cum[BQ - 1:BQ, :]
        is_fox, is_beta, is_g = _gate_lanes(z.shape)
        o_ref[...] = jnp.where(is_fox, cum, jnp.where(is_beta, _sigmoid(z), jnp.where(is_g, -jnp.exp(a_ref[...]) * sp, 0.0)))

    return pl.pallas_call(
        body, name="gates", grid=(NQ,),
        in_specs=[pl.BlockSpec((BQ, LANES), lambda i: (i, BLK_SMALL)), _vec_spec(LANES), _vec_spec(LANES)],
        out_specs=pl.BlockSpec((BQ, LANES), lambda i: (i, 0)), out_shape=jax.ShapeDtypeStruct((S, LANES), F32),
        scratch_shapes=[pltpu.VMEM((1, LANES), F32)], compiler_params=_cparams(),
    )(proj, bias_vec, alog_vec)


def _gates_bwd(proj, bias_vec, alog_vec, dgates_gdn, dcum_fox, dproj):
    def body(s_ref, b_ref, a_ref, dg_ref, dc_ref, dproj_in, dproj_ref, red_ref, carry_ref):
        del dproj_in
        i = pl.program_id(0)

        @pl.when(i == 0)
        def _():
            carry_ref[...] = jnp.zeros_like(carry_ref)
            red_ref[...] = jnp.zeros_like(red_ref)

        z = s_ref[...] + b_ref[...]
        dg = dg_ref[...] + dc_ref[...]
        r = lax.broadcasted_iota(jnp.int32, (BQ, BQ), 0)
        c = lax.broadcasted_iota(jnp.int32, (BQ, BQ), 1)
        upper = (c >= r).astype(F32)
        dlf = _hdot(upper, dg) + carry_ref[...]
        carry_ref[...] = dlf[0:1, :]
        sig = _sigmoid(z)
        g_scale = -jnp.exp(a_ref[...])
        is_fox, is_beta, is_g = _gate_lanes(z.shape)
        ds = jnp.where(is_fox, dlf * (1.0 - sig), jnp.where(is_beta, dg * sig * (1.0 - sig), jnp.where(is_g, dg * g_scale * sig, 0.0)))
        dproj_ref[:, 0:LANES] = ds.astype(BF16)
        dproj_ref[:, LANES:2 * LANES] = jnp.zeros((BQ, LANES), BF16)
        dalog = jnp.where(is_g, dg * g_scale * _softplus(z), 0.0)
        sums = jnp.sum(ds, axis=0, keepdims=True)
        red_ref[0:1, :] += jnp.where(is_fox[0:1], sums, 0.0)
        red_ref[1:2, :] += pltpu.roll(jnp.where(is_g[0:1], sums, 0.0), LANES - LANE_G, 1)
        red_ref[2:3, :] += pltpu.roll(jnp.sum(dalog, axis=0, keepdims=True), LANES - LANE_G, 1)

    blk = pl.BlockSpec((BQ, LANES), lambda i: (NQ - 1 - i, 0))
    return pl.pallas_call(
        body, name="gates_bwd", grid=(NQ,),
        in_specs=[pl.BlockSpec((BQ, LANES), lambda i: (NQ - 1 - i, BLK_SMALL)), _vec_spec(LANES), _vec_spec(LANES), blk, blk,
                  pl.BlockSpec(memory_space=pl.ANY)],
        out_specs=[pl.BlockSpec((BQ, 2 * LANES), lambda i: (NQ - 1 - i, BLK_SMALL // 2)), pl.BlockSpec((8, LANES), lambda i: (0, 0))],
        out_shape=[jax.ShapeDtypeStruct((S, DPROJ_PAD), BF16), jax.ShapeDtypeStruct((8, LANES), F32)],
        input_output_aliases={5: 0},
        scratch_shapes=[pltpu.VMEM((1, LANES), F32)], compiler_params=_cparams(),
    )(proj, bias_vec, alog_vec, dgates_gdn, dcum_fox, dproj)


FOX_SCALE = FHD ** -0.5
FOX_PAIRS = 2
FOX_PAIRS_BWD = 2


def _head_mask(e):
    lane = lax.broadcasted_iota(jnp.int32, (1, LANES), 1)
    return (lane >= e * FHD) & (lane < (e + 1) * FHD)


def _lane_col(vals, index):
    lane = lax.broadcasted_iota(jnp.int32, vals.shape, 1)
    return jnp.sum(jnp.where(lane == index, vals, 0.0), axis=1, keepdims=True)


def _sublane_row(vals, index):
    row = lax.broadcasted_iota(jnp.int32, vals.shape, 0)
    return jnp.sum(jnp.where(row == index, vals, 0.0), axis=0, keepdims=True)


def _pair_cols(c0, c1):
    lane = lax.broadcasted_iota(jnp.int32, (c0.shape[0], 2), 1)
    return jnp.where(lane == 0, c0, c1)


def _split3(x):
    hi = x.astype(BF16).astype(F32)
    rest = x - hi
    mid = rest.astype(BF16).astype(F32)
    return hi, mid, (rest - mid).astype(BF16).astype(F32)


def _fox_operand(vals, e, cum, is_query):
    lane = lax.broadcasted_iota(jnp.int32, (1, LANES), 1)
    base = (1 - e) * FHD
    parts = _split3(cum)
    own = jnp.where(_head_mask(e), vals * FOX_SCALE if is_query else vals, 0.0)
    cum_at, ones_at = (base, base + 3) if is_query else (base + 3, base)
    sign = 1.0 if is_query else -1.0
    out = own + jnp.where((lane >= ones_at) & (lane < ones_at + 3), 1.0, 0.0)
    for i, part in enumerate(parts):
        out = out + jnp.where(lane == cum_at + i, sign * part, 0.0)
    return out.astype(BF16)


def _causal_block():
    return lax.broadcasted_iota(jnp.int32, (BQ, BQ), 1) <= lax.broadcasted_iota(jnp.int32, (BQ, BQ), 0)


def _head_rms(o, masks):
    o2 = o * o
    r = [lax.rsqrt(jnp.sum(jnp.where(mk, o2, 0.0), axis=1, keepdims=True) * (1.0 / FHD) + EPS) for mk in masks]
    return jnp.where(masks[0], r[0], r[1])


def _hosted(exchange):
    if exchange is None:
        return [], [], [], [], []
    return (exchange.inputs, [HBM] * len(exchange.inputs), [HBM] * len(exchange.out_shape), exchange.out_shape,
            exchange.sem_shapes())


def _fox_fwd(proj, gates, w2, exchange=None):
    ex_in, ex_in_specs, ex_out_specs, ex_out_shape, ex_scratch = _hosted(exchange)

    n_in = 3 * FOX_PAIRS + 2
    heads = [(pp, e) for pp in range(FOX_PAIRS) for e in range(2)]

    def body(*refs):
        qkv_refs, g_ref, w_ref = refs[:3 * FOX_PAIRS], refs[3 * FOX_PAIRS], refs[3 * FOX_PAIRS + 1]
        mix_ref, o_ref, lse_ref = refs[n_in + len(ex_in):n_in + 3 + len(ex_in)]
        ka_ref, vb_ref = refs[n_in + 3 + len(ex_in) + len(ex_out_shape):n_in + 5 + len(ex_in) + len(ex_out_shape)]
        ex_refs = refs[n_in:n_in + len(ex_in)] + refs[n_in + 3 + len(ex_in):n_in + 3 + len(ex_in) + len(ex_out_shape)] + refs[-2:]
        grp, qi = pl.program_id(0), pl.program_id(1)

        def head_index(pp, e):
            return 2 * (FOX_PAIRS * grp + pp) + e

        if exchange is not None:
            @pl.when((grp == 0) & (qi == 0))
            def _():
                exchange.start(*exchange.split(ex_refs))

        @pl.when(qi == 0)
        def _():
            gt = g_ref[...]
            for pp in range(FOX_PAIRS):
                kv = qkv_refs[3 * pp + 1][...]
                for e in range(2):
                    ka_ref[2 * pp + e] = _fox_operand(kv, e, _lane_col(gt, head_index(pp, e)), False)
                vb_ref[pp] = qkv_refs[3 * pp + 2][...].astype(BF16)

        masks = [_head_mask(0), _head_mask(1)]
        gt = g_ref[pl.ds(pl.multiple_of(qi * BQ, BQ), BQ), :]
        qs = [_fox_operand(qkv_refs[3 * pp][...], e, _lane_col(gt, head_index(pp, e)), True) for pp, e in heads]
        n = range(len(heads))

        def block(kj, carry, diagonal):
            rows = pl.ds(pl.multiple_of(kj * BQ, BQ), BQ)
            s = [_dot(qs[i], ka_ref[i, rows, :], 1, 1) for i in n]
            if diagonal:
                s = [jnp.where(_causal_block(), s[i], -jnp.inf) for i in n]
            m_new = [jnp.maximum(carry[i][0], jnp.max(s[i], axis=-1, keepdims=True)) for i in n]
            p = [jnp.exp(s[i] - m_new[i]) for i in n]
            alpha = [jnp.exp(carry[i][0] - m_new[i]) for i in n]
            l_new = [alpha[i] * carry[i][1] + jnp.sum(p[i], axis=-1, keepdims=True) for i in n]
            pv = [_dot(p[i], vb_ref[heads[i][0], rows, :]) for i in n]
            return tuple((m_new[i], l_new[i], alpha[i] * carry[i][2] + pv[i]) for i in n)

        one = (jnp.full((BQ, 1), -jnp.inf, F32), jnp.zeros((BQ, 1), F32), jnp.zeros((BQ, LANES), F32))
        below = lax.fori_loop(0, qi, lambda kj, carry: block(kj, carry, False), (one,) * len(heads))
        done = block(qi, below, True)
        for pp in range(FOX_PAIRS):
            (m0, l0, a0), (m1, l1, a1) = done[2 * pp], done[2 * pp + 1]
            o = jnp.where(masks[0], a0 / l0, a1 / l1)
            cols = slice(pp * LANES, (pp + 1) * LANES)
            o_ref[:, cols] = o
            mix_ref[:, cols] = (o * _head_rms(o, masks) * w_ref[...]).astype(BF16)
            lse_ref[pp] = _pair_cols(m0 + jnp.log(l0), m1 + jnp.log(l1))

        if exchange is not None:
            @pl.when((grp == NPAIR // FOX_PAIRS // 2) & (qi == 0))
            def _():
                exchange.middle(*exchange.split(ex_refs))

            @pl.when((grp == NPAIR // FOX_PAIRS - 1) & (qi == NQ - 1))
            def _():
                exchange.rest(*exchange.split(ex_refs))

    qkv_specs = []
    for pp in range(FOX_PAIRS):
        qkv_specs.append(pl.BlockSpec((BQ, LANES), lambda g, i, pp=pp: (i, 3 * (FOX_PAIRS * g + pp))))
        qkv_specs.append(pl.BlockSpec((S, LANES), lambda g, i, pp=pp: (0, 3 * (FOX_PAIRS * g + pp) + 1)))
        qkv_specs.append(pl.BlockSpec((S, LANES), lambda g, i, pp=pp: (0, 3 * (FOX_PAIRS * g + pp) + 2)))
    blk = pl.BlockSpec((BQ, FOX_PAIRS * LANES), lambda g, i: (i, g))
    res = pl.pallas_call(
        body, name="fox_fwd", grid=(NPAIR // FOX_PAIRS, NQ),
        in_specs=qkv_specs + [pl.BlockSpec((S, LANES), lambda g, i: (0, 0)), pl.BlockSpec((1, LANES), lambda g, i: (0, 0))]
        + ex_in_specs,
        out_specs=[blk, blk, pl.BlockSpec((FOX_PAIRS, BQ, 2), lambda g, i: (g, i, 0))] + ex_out_specs,
        out_shape=[jax.ShapeDtypeStruct((S, D), BF16), jax.ShapeDtypeStruct((S, DFOX), F32),
                   jax.ShapeDtypeStruct((NPAIR, S, 2), F32)] + ex_out_shape,
        scratch_shapes=[pltpu.VMEM((2 * FOX_PAIRS, S, LANES), BF16), pltpu.VMEM((FOX_PAIRS, S, LANES), BF16)] + ex_scratch,
        compiler_params=_cparams(),
    )(*([proj] * (3 * FOX_PAIRS)), gates, w2, *ex_in)
    return res[0], res[1], res[2], res[3:]


def _fox_norm_bwd(o, dmix, w2):
    def body(o_ref, g_ref, w_ref, do_ref, dl_ref, dw_ref):
        hp, qi = pl.program_id(0), pl.program_id(1)
        masks = [_head_mask(0), _head_mask(1)]
        ov = o_ref[...]
        g = g_ref[...]
        r = _head_rms(ov, masks)
        gw = g * w_ref[...]
        gwo = gw * ov
        mean = [jnp.sum(jnp.where(mk, gwo, 0.0), axis=1, keepdims=True) * (1.0 / FHD) for mk in masks]
        do = r * gw - ov * (r * r * r) * jnp.where(masks[0], mean[0], mean[1])
        do_ref[...] = do.astype(BF16)
        doo = do * ov
        dl_ref[...] = _pair_cols(*[jnp.sum(jnp.where(mk, doo, 0.0), axis=1, keepdims=True) for mk in masks])

        @pl.when((hp == 0) & (qi == 0))
        def _():
            dw_ref[...] = jnp.zeros_like(dw_ref)

        dw_ref[...] += jnp.sum(g * ov * r, axis=0, keepdims=True)

        @pl.when((hp == NPAIR - 1) & (qi == NQ - 1))
        def _():
            dw = dw_ref[...]
            dw_ref[...] = dw + pltpu.roll(dw, FHD, 1)

    blk = pl.BlockSpec((BQ, LANES), lambda hp, i: (i, hp))
    vec = pl.BlockSpec((1, LANES), lambda hp, i: (0, 0))
    return pl.pallas_call(
        body, name="fox_norm_bwd", grid=(NPAIR, NQ), in_specs=[blk, blk, vec],
        out_specs=[blk, pl.BlockSpec((None, BQ, 2), lambda hp, i: (hp, i, 0)), vec],
        out_shape=[jax.ShapeDtypeStruct((S, DFOX), BF16), jax.ShapeDtypeStruct((NPAIR, S, 2), F32),
                   jax.ShapeDtypeStruct((1, LANES), F32)],
        compiler_params=_cparams(),
    )(o, dmix, w2)


def _fox_bwd(proj, do, gates, lse, delta, exchange=None):
    ex_in, ex_in_specs, ex_out_specs, ex_out_shape, ex_scratch = _hosted(exchange)

    pg = FOX_PAIRS_BWD
    n_in = 3 * pg + 4
    heads = [(pp, e) for pp in range(pg) for e in range(2)]

    def body(*refs):
        qkv_refs = refs[:3 * pg]
        do_ref, g_ref, lse_ref, dl_ref = refs[3 * pg:n_in]
        dproj_ref, dc_ref = refs[n_in + len(ex_in):n_in + 2 + len(ex_in)]
        qa_ref, dq_ref = refs[n_in + 2 + len(ex_in) + len(ex_out_shape):n_in + 4 + len(ex_in) + len(ex_out_shape)]
        ex_refs = refs[n_in:n_in + len(ex_in)] + refs[n_in + 2 + len(ex_in):n_in + 2 + len(ex_in) + len(ex_out_shape)] + refs[-2:]
        grp, kj = pl.program_id(0), pl.program_id(1)

        def head_index(pp, e):
            return 2 * (pg * grp + pp) + e

        if exchange is not None:
            @pl.when((grp == 0) & (kj == 0))
            def _():
                exchange.start(*exchange.split(ex_refs))

        @pl.when(kj == 0)
        def _():
            gt = g_ref[...]
            for pp in range(pg):
                qv = qkv_refs[3 * pp][...]
                for e in range(2):
                    qa_ref[2 * pp + e] = _fox_operand(qv, e, _lane_col(gt, head_index(pp, e)), True)
            dq_ref[...] = jnp.zeros_like(dq_ref)

        @pl.when((grp == 0) & (kj == 0))
        def _():
            dc_ref[...] = jnp.zeros_like(dc_ref)

        masks = [_head_mask(0), _head_mask(1)]
        krows = pl.ds(pl.multiple_of(kj * BQ, BQ), BQ)
        gk = g_ref[krows, :]
        kas = [_fox_operand(qkv_refs[3 * pp + 1][...], e, _lane_col(gk, head_index(pp, e)), False) for pp, e in heads]
        vbs = [qkv_refs[3 * pp + 2][...].astype(BF16) for pp in range(pg)]
        lane = lax.broadcasted_iota(jnp.int32, (BQ, LANES), 1)
        n = range(len(heads))

        def block(qi, carry, diagonal):
            dks, dvs, css = carry
            rows = pl.ds(pl.multiple_of(qi * BQ, BQ), BQ)
            qa = [qa_ref[i, rows, :] for i in n]
            s = [_dot(qa[i], kas[i], 1, 1) for i in n]
            if diagonal:
                s = [jnp.where(_causal_block(), s[i], -jnp.inf) for i in n]
            dov = [do_ref[rows, pp * LANES:(pp + 1) * LANES] for pp in range(pg)]
            doe = [jnp.where(masks[e], dov[pp], jnp.zeros_like(dov[pp])) for pp, e in heads]
            lse2 = [lse_ref[pp, rows, :] for pp in range(pg)]
            dl2 = [dl_ref[pp, rows, :] for pp in range(pg)]
            p = [jnp.exp(s[i] - _lane_col(lse2[heads[i][0]], heads[i][1])) for i in n]
            dp = [_dot(doe[i], vbs[heads[i][0]], 1, 1) for i in n]
            ds = [p[i] * (dp[i] - _lane_col(dl2[heads[i][0]], heads[i][1])) for i in n]
            dv_part = [_dot(p[i], doe[i], 0, 0) for i in n]
            dk_part = [_dot(ds[i], jnp.where(masks[heads[i][1]], qa[i], jnp.zeros_like(qa[i])), 0, 0) for i in n]
            dq_part = [jnp.where(masks[heads[i][1]], _dot(ds[i], kas[i]), 0.0) for i in n]
            css = tuple(css[i] + jnp.sum(ds[i], axis=0, keepdims=True) for i in n)
            dc = jnp.zeros((BQ, LANES), F32)
            for i in n:
                dc = dc + jnp.where(lane == head_index(*heads[i]), jnp.sum(ds[i], axis=1, keepdims=True), 0.0)
            for pp in range(pg):
                dq_ref[pp, rows, :] += (dq_part[2 * pp] + dq_part[2 * pp + 1]) * FOX_SCALE
            dc_ref[rows, :] += dc
            dks = tuple(dks[pp] + dk_part[2 * pp] + dk_part[2 * pp + 1] for pp in range(pg))
            dvs = tuple(dvs[pp] + dv_part[2 * pp] + dv_part[2 * pp + 1] for pp in range(pg))
            return dks, dvs, css

        zero = jnp.zeros((BQ, LANES), F32)
        first = block(kj, ((zero,) * pg, (zero,) * pg, (jnp.zeros((1, BQ), F32),) * len(heads)), True)
        dks, dvs, css = lax.fori_loop(kj + 1, NQ, lambda qi, carry: block(qi, carry, False), first)
        r = lax.broadcasted_iota(jnp.int32, (BQ, BQ), 0)
        c = lax.broadcasted_iota(jnp.int32, (BQ, BQ), 1)
        dcol = jnp.zeros((BQ, LANES), F32)
        for i in n:
            col = jnp.sum(jnp.where(r == c, css[i], 0.0), axis=1, keepdims=True)
            dcol = dcol + jnp.where(lane == head_index(*heads[i]), col, 0.0)
        dc_ref[krows, :] -= dcol
        for pp in range(pg):
            base = 3 * pp * LANES
            dproj_ref[krows, base + LANES:base + 2 * LANES] = dks[pp].astype(BF16)
            dproj_ref[krows, base + 2 * LANES:base + 3 * LANES] = dvs[pp].astype(BF16)

        @pl.when(kj == NQ - 1)
        def _():
            for pp in range(pg):
                dproj_ref[:, 3 * pp * LANES:(3 * pp + 1) * LANES] = dq_ref[pp].astype(BF16)

        if exchange is not None:
            @pl.when((grp == NPAIR // pg // 2) & (kj == 0))
            def _():
                exchange.middle(*exchange.split(ex_refs))

            @pl.when((grp == NPAIR // pg - 1) & (kj == NQ - 1))
            def _():
                exchange.rest(*exchange.split(ex_refs))

    qkv_specs = []
    for pp in range(pg):
        qkv_specs.append(pl.BlockSpec((S, LANES), lambda g, j, pp=pp: (0, 3 * (pg * g + pp))))
        qkv_specs.append(pl.BlockSpec((BQ, LANES), lambda g, j, pp=pp: (j, 3 * (pg * g + pp) + 1)))
        qkv_specs.append(pl.BlockSpec((BQ, LANES), lambda g, j, pp=pp: (j, 3 * (pg * g + pp) + 2)))
    pair = pl.BlockSpec((pg, S, 2), lambda g, j: (g, 0, 0))
    res = pl.pallas_call(
        body, name="fox_bwd", grid=(NPAIR // pg, NQ),
        in_specs=qkv_specs + [pl.BlockSpec((S, pg * LANES), lambda g, j: (0, g)), pl.BlockSpec((S, LANES), lambda g, j: (0, 0)),
                              pair, pair] + ex_in_specs,
        out_specs=[pl.BlockSpec((S, 3 * pg * LANES), lambda g, j: (0, g)), pl.BlockSpec((S, LANES), lambda g, j: (0, 0))]
        + ex_out_specs,
        out_shape=[jax.ShapeDtypeStruct((S, DPROJ_PAD), BF16), jax.ShapeDtypeStruct((S, LANES), F32)] + ex_out_shape,
        scratch_shapes=[pltpu.VMEM((2 * pg, S, LANES), BF16), pltpu.VMEM((pg, S, LANES), F32)] + ex_scratch,
        compiler_params=_cparams(),
    )(*([proj] * (3 * pg)), do, gates, lse, delta, *ex_in)
    return res[0], res[1], res[2:]


NQKV = 3 * NGH
GDN_QSCALE = GHD ** -0.5


def _shift_down(x, s):
    if s == 0:
        return x
    row = lax.broadcasted_iota(jnp.int32, x.shape, 0)
    return jnp.where(row >= s, pltpu.roll(x, s, 0), 0.0)


def _shift_up(x, s):
    if s == 0:
        return x
    n = x.shape[0]
    row = lax.broadcasted_iota(jnp.int32, x.shape, 0)
    return jnp.where(row < n - s, pltpu.roll(x, n - s, 0), 0.0)


def _conv_taps(xv):
    return [_shift_down(xv, CONV_K - 1 - j) for j in range(CONV_K)]


def _conv_pre(taps, wv):
    pre = taps[CONV_K - 1] * wv[CONV_K - 1:CONV_K, :]
    for j in range(CONV_K - 1):
        pre = pre + taps[j] * wv[j:j + 1, :]
    return pre


def _l2_factors(b):
    return b < 2 * NGH, jnp.where(b < NGH, GDN_QSCALE, 1.0)


def _gdn_pre(proj, conv_w):
    def body(x_ref, w_ref, o_ref):
        b = pl.program_id(0)
        c = _silu(_conv_pre(_conv_taps(x_ref[...]), w_ref[...]))
        normed, scale = _l2_factors(b)
        rs = lax.rsqrt(jnp.sum(c * c, axis=-1, keepdims=True) + EPS)
        o_ref[...] = c * jnp.where(normed, rs, 1.0) * scale

    return pl.pallas_call(
        body, name="gdn_pre", grid=(NQKV,),
        in_specs=[pl.BlockSpec((S, GHD), lambda b: (0, BLK_GDN + b)), pl.BlockSpec((CONV_K, GHD), lambda b: (0, b))],
        out_specs=pl.BlockSpec((S, GHD), lambda b: (0, b)),
        out_shape=jax.ShapeDtypeStruct((S, NQKV * GHD), F32), compiler_params=_cparams(),
    )(proj, conv_w)


def _gdn_pre_bwd(proj, conv_w, dqkv, dproj):
    def body(x_ref, w_ref, dy_ref, dproj_in, dx_ref, dw_ref):
        del dproj_in
        b = pl.program_id(0)
        taps = _conv_taps(x_ref[...])
        wv = w_ref[...]
        pre = _conv_pre(taps, wv)
        sig = _sigmoid(pre)
        c = pre * sig
        normed, scale = _l2_factors(b)
        g = dy_ref[...] * scale
        rs = lax.rsqrt(jnp.sum(c * c, axis=-1, keepdims=True) + EPS)
        dc_n = rs * g - c * (rs * rs * rs) * jnp.sum(g * c, axis=-1, keepdims=True)
        dc = jnp.where(normed, dc_n, g)
        dpre = dc * sig * (1.0 + pre * (1.0 - sig))
        dx = dpre * wv[CONV_K - 1:CONV_K, :]
        for j in range(CONV_K - 1):
            dx = dx + _shift_up(dpre, CONV_K - 1 - j) * wv[j:j + 1, :]
        dx_ref[...] = dx.astype(BF16)
        for j in range(CONV_K):
            dw_ref[j:j + 1, :] = jnp.sum(dpre * taps[j], axis=0, keepdims=True)

    return pl.pallas_call(
        body, name="gdn_pre_bwd", grid=(NQKV,),
        in_specs=[pl.BlockSpec((S, GHD), lambda b: (0, BLK_GDN + b)), pl.BlockSpec((CONV_K, GHD), lambda b: (0, b)),
                  pl.BlockSpec((None, S, GHD), lambda b: (b // NGH, 0, b % NGH)), pl.BlockSpec(memory_space=pl.ANY)],
        out_specs=[pl.BlockSpec((S, GHD), lambda b: (0, BLK_GDN + b)), pl.BlockSpec((CONV_K, GHD), lambda b: (0, b))],
        out_shape=[jax.ShapeDtypeStruct((S, DPROJ_PAD), BF16), jax.ShapeDtypeStruct((CONV_K, NQKV * GHD), F32)],
        input_output_aliases={3: 0}, compiler_params=_cparams(),
    )(proj, conv_w, dqkv, dproj)


CB = 16
NCB = NCH // CB


def _chunk_prep(qs, ks, vs, gcols, bcols, t_saved=None):
    n = range(len(qs))
    r = lax.broadcasted_iota(jnp.int32, (CHUNK, CHUNK), 0)
    c = lax.broadcasted_iota(jnp.int32, (CHUNK, CHUNK), 1)
    incl = c <= r
    eye = (r == c).astype(F32)
    grow = [jnp.sum(gcols[i] * eye, axis=0, keepdims=True) for i in n]
    gc_col = [jnp.sum(jnp.where(incl, grow[i], 0.0), axis=1, keepdims=True) for i in n]
    gc_row = [jnp.sum(jnp.where(r <= c, gcols[i], 0.0), axis=0, keepdims=True) for i in n]
    decay = [jnp.exp(jnp.where(incl, gc_col[i] - gc_row[i], -jnp.inf)) for i in n]
    kb = [ks[i] * bcols[i] for i in n]
    vb = [vs[i] * bcols[i] for i in n]
    kk = [_mm_nt(kb[i], ks[i]) for i in n]
    m = [jnp.where(c < r, kk[i] * decay[i], 0.0) for i in n]
    if t_saved is None:
        t_inv = [eye - m[i] for i in n]
        p = [_dot3(m[i], m[i]) for i in n]
        for step in range(5):
            t_inv = [t_inv[i] + _dot3(t_inv[i], p[i]) for i in n]
            if step < 4:
                p = [_dot3(p[i], p[i]) for i in n]
    else:
        t_inv = [_saved_inverse(m[i], t_saved[i]) for i in n]
    egc = [jnp.exp(gc_col[i]) for i in n]
    u = [_mm_nn(t_inv[i], vb[i]) for i in n]
    w = [_mm_nn(t_inv[i], kb[i] * egc[i]) for i in n]
    qk = [_mm_nt(qs[i], ks[i]) for i in n]
    gc_last = [gc_col[i][CHUNK - 1:CHUNK, :] for i in n]
    return [(u[i], w[i], qk[i] * decay[i], qs[i] * egc[i], ks[i] * jnp.exp(gc_last[i] - gc_col[i]), jnp.exp(gc_last[i]),
             t_inv[i]) for i in n]


def _prep_specs():
    rows = CB * CHUNK
    qs = pl.BlockSpec((rows, GHD), lambda i, h: (i, h))
    ks = pl.BlockSpec((rows, GHD), lambda i, h: (i, NGH + h))
    vs = pl.BlockSpec((rows, GHD), lambda i, h: (i, 2 * NGH + h))
    gs = pl.BlockSpec((rows, LANES), lambda i, h: (i, 0))
    a_s = pl.BlockSpec((None, rows, CHUNK), lambda i, h: (h, i, 0))
    gl_s = pl.BlockSpec((None, CB, 1, LANES), lambda i, h: (h, i, 0, 0))
    return qs, ks, vs, gs, a_s, gl_s


def _gdn_prep(qkv, gates, exchange=None):
    ex_in, ex_in_specs, ex_out_specs, ex_out_shape, ex_scratch = _hosted(exchange)

    def body(*refs):
        q_ref, k_ref, v_ref, g_ref = refs[:4]
        u_ref, w_ref, qd_ref, kd_ref, a_ref, gl_ref, t_ref = refs[4 + len(ex_in):11 + len(ex_in)]
        ex_refs = refs[4:4 + len(ex_in)] + refs[11 + len(ex_in):]
        h = pl.program_id(1)

        if exchange is not None:
            @pl.when((pl.program_id(0) == 0) & (h == 0))
            def _():
                exchange.start(*exchange.split(ex_refs))

        chunks = [pl.ds(cidx * CHUNK, CHUNK) for cidx in range(CB)]
        gts = [g_ref[rows, :] for rows in chunks]
        outs = _chunk_prep([q_ref[rows, :] for rows in chunks], [k_ref[rows, :] for rows in chunks],
                           [v_ref[rows, :] for rows in chunks], [_lane_col(gt, LANE_G + h) for gt in gts],
                           [_lane_col(gt, LANE_BETA + h) for gt in gts])
        for cidx, rows in enumerate(chunks):
            u, w, a, qd, kd, gl, t_inv = outs[cidx]
            u_ref[rows, :] = u
            w_ref[rows, :] = w
            qd_ref[rows, :] = qd
            kd_ref[rows, :] = kd
            a_ref[rows, :] = a
            t_ref[rows, :] = t_inv
            gl_ref[cidx] = jnp.broadcast_to(gl, (1, LANES))

        if exchange is not None:
            @pl.when((pl.program_id(0) == NCB // 2) & (h == 0))
            def _():
                exchange.middle(*exchange.split(ex_refs))

            @pl.when((pl.program_id(0) == NCB - 1) & (h == NGH - 1))
            def _():
                exchange.rest(*exchange.split(ex_refs))

    qs, ks, vs, gs, a_s, gl_s = _prep_specs()
    tok = jax.ShapeDtypeStruct((S, DGDN), F32)
    sq = jax.ShapeDtypeStruct((NGH, S, CHUNK), F32)
    res = pl.pallas_call(
        body, name="gdn_prep", grid=(NCB, NGH), in_specs=[qs, ks, vs, gs] + ex_in_specs,
        out_specs=[qs, qs, qs, qs, a_s, gl_s, a_s] + ex_out_specs,
        out_shape=[tok, tok, tok, tok, sq, jax.ShapeDtypeStruct((NGH, NCH, 1, LANES), F32), sq] + ex_out_shape,
        scratch_shapes=ex_scratch, compiler_params=_cparams(),
    )(qkv, qkv, qkv, gates, *ex_in)
    return res[:7], res[7:]


def _gdn_prep_bwd(qkv, gates, t_inv, du, dw, dqd, dkd, da, dgl, exchange=None):
    ex_in, ex_in_specs, ex_out_specs, ex_out_shape, ex_scratch = _hosted(exchange)

    def body(*refs):
        q_ref, k_ref, v_ref, g_ref, t_ref, du_ref, dw_ref, dqd_ref, dkd_ref, da_ref, dgl_ref = refs[:11]
        dqkv_ref, dg_ref = refs[11 + len(ex_in):13 + len(ex_in)]
        ex_refs = refs[11:11 + len(ex_in)] + refs[13 + len(ex_in):]
        h = pl.program_id(1)

        if exchange is not None:
            @pl.when((pl.program_id(0) == 0) & (h == 0))
            def _():
                exchange.start(*exchange.split(ex_refs))

        @pl.when(h == 0)
        def _():
            dg_ref[...] = jnp.zeros_like(dg_ref)

        lane = lax.broadcasted_iota(jnp.int32, (CHUNK, LANES), 1)
        chunks = [pl.ds(cidx * CHUNK, CHUNK) for cidx in range(CB)]
        gts = [g_ref[rows, :] for rows in chunks]
        t_saved = [t_ref[rows, :] for rows in chunks]
        _, vjp = jax.vjp(lambda *args: [o[:6] for o in _chunk_prep(*args, t_saved=t_saved)],
                         [q_ref[rows, :] for rows in chunks], [k_ref[rows, :] for rows in chunks],
                         [v_ref[rows, :] for rows in chunks], [_lane_col(gt, LANE_G + h) for gt in gts],
                         [_lane_col(gt, LANE_BETA + h) for gt in gts])
        dqs, dks, dvs, dgcs, dbcs = vjp([(du_ref[rows, :], dw_ref[rows, :], da_ref[rows, :], dqd_ref[rows, :],
                                          dkd_ref[rows, :], dgl_ref[cidx][:, 0:1]) for cidx, rows in enumerate(chunks)])
        for cidx, rows in enumerate(chunks):
            dq, dk, dv, dgc, dbc = dqs[cidx], dks[cidx], dvs[cidx], dgcs[cidx], dbcs[cidx]
            dqkv_ref[0, rows, :] = dq
            dqkv_ref[1, rows, :] = dk
            dqkv_ref[2, rows, :] = dv
            dg_ref[rows, :] += jnp.where(lane == LANE_G + h, dgc, 0.0) + jnp.where(lane == LANE_BETA + h, dbc, 0.0)

        if exchange is not None:
            @pl.when((pl.program_id(0) == NCB - 1) & (h == NGH - 1))
            def _():
                exchange.finish(*exchange.split(ex_refs))

    qs, ks, vs, gs, a_s, gl_s = _prep_specs()
    res = pl.pallas_call(
        body, name="gdn_prep_bwd", grid=(NCB, NGH), in_specs=[qs, ks, vs, gs, a_s, qs, qs, qs, qs, a_s, gl_s] + ex_in_specs,
        out_specs=[pl.BlockSpec((3, CB * CHUNK, GHD), lambda i, h: (0, i, h)), gs] + ex_out_specs,
        out_shape=[jax.ShapeDtypeStruct((3, S, DGDN), F32), jax.ShapeDtypeStruct((S, LANES), F32)] + ex_out_shape,
        scratch_shapes=ex_scratch, compiler_params=_cparams(),
    )(qkv, qkv, qkv, gates, t_inv, du, dw, dqd, dkd, da, dgl, *ex_in)
    return res[0], res[1], res[2:]


def _scan_specs(nh, parts, reverse):
    wide, rows, chunks = nh * GHD, S // parts, NCH // parts

    def part(p):
        return parts - 1 - p if reverse else p

    hs = pl.BlockSpec((rows, wide), lambda g, p: (part(p), g))
    a_s = pl.BlockSpec((nh, rows, CHUNK), lambda g, p: (g, part(p), 0))
    gl_s = pl.BlockSpec((nh, chunks, 1, LANES), lambda g, p: (g, part(p), 0, 0))
    st_s = pl.BlockSpec((nh, chunks, GHD, GHD), lambda g, p: (g, part(p), 0, 0))
    gz_s = pl.BlockSpec((rows, wide), lambda g, p: (part(p), BLK_GZ // nh + g))
    mix_s = pl.BlockSpec((rows, wide), lambda g, p: (part(p), NPAIR // nh + g))
    return hs, a_s, gl_s, st_s, gz_s, mix_s


def _head_cols(hh):
    return slice(hh * GHD, (hh + 1) * GHD)


SCAN_HEADS, SCAN_PARTS = 4, 2
SCAN_HEADS_BWD, SCAN_PARTS_BWD = 4, 4


def _gdn_scan(u, w, qd, kd, a, gl, proj, w_norm, mix):
    heads = range(SCAN_HEADS)

    def body(u_ref, w_ref, qd_ref, kd_ref, a_ref, gl_ref, z_ref, wn_ref, mix_in, mix_ref, o_ref, st_ref, carry_ref):
        del mix_in

        @pl.when(pl.program_id(1) == 0)
        def _():
            carry_ref[...] = jnp.zeros_like(carry_ref)

        def step(ci, states):
            rows = pl.ds(pl.multiple_of(ci * CHUNK, CHUNK), CHUNK)
            for hh in heads:
                st_ref[hh, ci] = states[hh]
            ws = [_dot(w_ref[rows, _head_cols(hh)], states[hh]) for hh in heads]
            qs = [_dot(qd_ref[rows, _head_cols(hh)], states[hh]) for hh in heads]
            vn = [u_ref[rows, _head_cols(hh)] - ws[hh] for hh in heads]
            av = [_dot(a_ref[hh, rows, :], vn[hh]) for hh in heads]
            kv = [_dot(kd_ref[rows, _head_cols(hh)], vn[hh], 0, 0) for hh in heads]
            for hh in heads:
                o_ref[rows, _head_cols(hh)] = qs[hh] + av[hh]
            return tuple(states[hh] * gl_ref[hh, ci] + kv[hh] for hh in heads)

        last = lax.fori_loop(0, NCH // SCAN_PARTS, step, tuple(carry_ref[hh] for hh in heads))
        for hh in heads:
            carry_ref[hh] = last[hh]
            ov = o_ref[:, _head_cols(hh)]
            mix_ref[:, _head_cols(hh)] = (ov * _rms_scale(ov) * wn_ref[...] * _silu(z_ref[:, _head_cols(hh)])).astype(BF16)

    hs, a_s, gl_s, st_s, gz_s, mix_s = _scan_specs(SCAN_HEADS, SCAN_PARTS, False)
    return pl.pallas_call(
        body, name="gdn_scan", grid=(NGH // SCAN_HEADS, SCAN_PARTS),
        in_specs=[hs, hs, hs, hs, a_s, gl_s, gz_s, pl.BlockSpec((1, GHD), lambda g, p: (0, 0)),
                  pl.BlockSpec(memory_space=pl.ANY)],
        out_specs=[mix_s, hs, st_s],
        out_shape=[jax.ShapeDtypeStruct((S, D), BF16), jax.ShapeDtypeStruct((S, DGDN), F32),
                   jax.ShapeDtypeStruct((NGH, NCH, GHD, GHD), F32)],
        input_output_aliases={8: 0}, scratch_shapes=[pltpu.VMEM((SCAN_HEADS, GHD, GHD), F32)], compiler_params=_cparams(),
    )(u, w, qd, kd, a, gl, proj, w_norm, mix)


def _gdn_scan_bwd(dmix, o, proj, w_norm, u, w, qd, kd, a, gl, states, dproj, exchange=None):
    ex_in, ex_in_specs, ex_out_specs, ex_out_shape, ex_scratch = _hosted(exchange)
    groups = NGH // SCAN_HEADS_BWD

    def body(*refs):
        dy_ref, o_ref, z_ref, wn_ref, u_ref, w_ref, qd_ref, kd_ref, a_ref, gl_ref, st_ref = refs[:11]
        dz_ref, du_ref, dw_ref, dqd_ref, dkd_ref, da_ref, dgl_ref, dwn_ref = refs[12 + len(ex_in):20 + len(ex_in)]
        do_ref, carry_ref = refs[20 + len(ex_in) + len(ex_out_shape):22 + len(ex_in) + len(ex_out_shape)]
        ex_refs = refs[12:12 + len(ex_in)] + refs[20 + len(ex_in):20 + len(ex_in) + len(ex_out_shape)] + refs[-2:]
        heads = range(SCAN_HEADS_BWD)
        chunks = NCH // SCAN_PARTS_BWD

        if exchange is not None:
            @pl.when((pl.program_id(0) == 0) & (pl.program_id(1) == 0))
            def _():
                exchange.start(*exchange.split(ex_refs))

        @pl.when((pl.program_id(0) == 0) & (pl.program_id(1) == 0))
        def _():
            dwn_ref[...] = jnp.zeros_like(dwn_ref)

        @pl.when(pl.program_id(1) == 0)
        def _():
            carry_ref[...] = jnp.zeros_like(carry_ref)

        wn = wn_ref[...]
        for hh in heads:
            c = _head_cols(hh)
            ov = o_ref[:, c]
            zv = z_ref[:, c]
            g = dy_ref[:, c]
            sig = _sigmoid(zv)
            dz_ref[:, c] = (g * (ov * _rms_scale(ov) * wn) * sig * (1.0 + zv * (1.0 - sig))).astype(BF16)
            do, dwt = _rms_bwd(ov, wn, g * zv * sig)
            do_ref[:, c] = do
            dwn_ref[...] += jnp.sum(dwt, axis=0, keepdims=True)

        def step(t, dstates):
            ci = chunks - 1 - t
            rows = pl.ds(pl.multiple_of(ci * CHUNK, CHUNK), CHUNK)
            cols = [_head_cols(hh) for hh in heads]
            state = [st_ref[hh, ci] for hh in heads]
            dov = [do_ref[rows, cols[hh]] for hh in heads]
            wv = [w_ref[rows, cols[hh]] for hh in heads]
            ws = [_dot(wv[hh], state[hh]) for hh in heads]
            adov = [_dot(a_ref[hh, rows, :], dov[hh], 0, 0) for hh in heads]
            kds = [_dot(kd_ref[rows, cols[hh]], dstates[hh]) for hh in heads]
            dqd = [_dot(dov[hh], state[hh], 1, 1) for hh in heads]
            qdo = [_dot(qd_ref[rows, cols[hh]], dov[hh], 0, 0) for hh in heads]
            vn = [u_ref[rows, cols[hh]] - ws[hh] for hh in heads]
            dvn = [adov[hh] + kds[hh] for hh in heads]
            da = [_dot(dov[hh], vn[hh], 1, 1) for hh in heads]
            dkd = [_dot(vn[hh], dstates[hh], 1, 1) for hh in heads]
            dwv = [_dot(dvn[hh], state[hh], 1, 1) for hh in heads]
            wdv = [_dot(wv[hh], dvn[hh], 0, 0) for hh in heads]
            for hh in heads:
                da_ref[hh, rows, :] = da[hh]
                dqd_ref[rows, cols[hh]] = dqd[hh]
                dkd_ref[rows, cols[hh]] = dkd[hh]
                dgl = jnp.sum(jnp.sum(dstates[hh] * state[hh], axis=1, keepdims=True), axis=0, keepdims=True)
                dgl_ref[hh, ci] = jnp.broadcast_to(dgl, (1, LANES))
                du_ref[rows, cols[hh]] = dvn[hh]
                dw_ref[rows, cols[hh]] = -dwv[hh]
            return tuple(dstates[hh] * gl_ref[hh, ci] + qdo[hh] - wdv[hh] for hh in heads)

        last = lax.fori_loop(0, chunks, step, tuple(carry_ref[hh] for hh in heads))
        for hh in heads:
            carry_ref[hh] = last[hh]

        if exchange is not None:
            @pl.when((pl.program_id(0) == groups - 1) & (pl.program_id(1) == SCAN_PARTS_BWD - 1))
            def _():
                exchange.finish(*exchange.split(ex_refs))

    hs, a_s, gl_s, st_s, gz_s, mix_s = _scan_specs(SCAN_HEADS_BWD, SCAN_PARTS_BWD, True)
    vec = pl.BlockSpec((1, GHD), lambda g, p: (0, 0))
    tok = jax.ShapeDtypeStruct((S, DGDN), F32)
    res = pl.pallas_call(
        body, name="gdn_scan_bwd", grid=(groups, SCAN_PARTS_BWD),
        in_specs=[mix_s, hs, gz_s, vec, hs, hs, hs, hs, a_s, gl_s, st_s, pl.BlockSpec(memory_space=pl.ANY)] + ex_in_specs,
        out_specs=[gz_s, hs, hs, hs, hs, a_s, gl_s, vec] + ex_out_specs,
        out_shape=[jax.ShapeDtypeStruct((S, DPROJ_PAD), BF16), tok, tok, tok, tok,
                   jax.ShapeDtypeStruct((NGH, S, CHUNK), F32), jax.ShapeDtypeStruct((NGH, NCH, 1, LANES), F32),
                   jax.ShapeDtypeStruct((1, GHD), F32)] + ex_out_shape,
        input_output_aliases={11: 0},
        scratch_shapes=[pltpu.VMEM((S // SCAN_PARTS_BWD, SCAN_HEADS_BWD * GHD), F32),
                        pltpu.VMEM((SCAN_HEADS_BWD, GHD, GHD), F32)] + ex_scratch,
        compiler_params=_cparams(),
    )(dmix, o, proj, w_norm, u, w, qd, kd, a, gl, states, dproj, *ex_in)
    return res[:8], res[8:]


def _place():
    return lax.axis_index("x"), lax.axis_index("y"), lax.axis_index("c")


def _other_chips(x, y):
    return [(1 - x, y), (x, 1 - y), (1 - x, 1 - y)]


HBM = pl.BlockSpec(memory_space=pltpu.HBM)
VMEM = pl.BlockSpec(memory_space=pltpu.VMEM)


def _half_rows(ref_or_rows, half):
    rows = ref_or_rows // 2
    return pl.ds(pl.multiple_of(half * rows, rows), rows)


class _Exchange:
    def __init__(self, inputs, out_shape, n_sems, start, finish=None, middle=None, rest=None):
        self.inputs, self.out_shape, self.n_sems, self.start = inputs, out_shape, n_sems, start
        if finish is None:
            def finish(*refs):
                middle(*refs)
                rest(*refs)
        self.finish = finish
        self.middle = middle if middle is not None else (lambda *refs: None)
        self.rest = rest if rest is not None else finish

    def sem_shapes(self):
        return [pltpu.SemaphoreType.DMA((self.n_sems,)), pltpu.SemaphoreType.DMA((self.n_sems,))]

    def split(self, refs):
        n_in, n_out = len(self.inputs), len(self.out_shape)
        return refs[:n_in], refs[n_in:n_in + n_out], refs[n_in + n_out], refs[n_in + n_out + 1]


def _allgather_exchange(shards, whole=()):
    n, nw = len(shards), len(whole)
    slots = 8

    def plan(src, outs, send_sems, recv_sems):
        x, y, c = _place()
        via_x, via_y, diagonal = _other_chips(x, y)
        id_x, id_y, id_diagonal = [2 * chip[0] + chip[1] for chip in (via_x, via_y, diagonal)]
        me, sibling = (x, y, c), (x, y, 1 - c)

        def rows_of(a, half, quarter):
            total = src[a].shape[0]
            if quarter is None:
                return _half_rows(total, half)
            return pl.ds(pl.multiple_of(half * (total // 2) + quarter * (total // 4), total // 4), total // 4)

        def copy(a, k, chip_index, half, quarter, to, from_src=False):
            rows = rows_of(a, half, quarter)
            dst = outs[a].at[chip_index, rows]
            return pltpu.make_async_remote_copy(
                src_ref=src[a].at[rows] if from_src else dst, dst_ref=dst, send_sem=send_sems.at[slots * a + k],
                recv_sem=recv_sems.at[slots * a + k], device_id=to, device_id_type=MESH)

        def whole_copy(b, k, chip_index, to):
            return pltpu.make_async_remote_copy(
                src_ref=src[n + b], dst_ref=outs[n + b].at[chip_index], send_sem=send_sems.at[slots * n + 3 * b + k],
                recv_sem=recv_sems.at[slots * n + 3 * b + k], device_id=to, device_id_type=MESH)

        first, stages, last = [], [], []
        for a in range(n):
            first += [copy(a, 0, 2 * x + y, c, None, (*via_x, c), True), copy(a, 1, 2 * x + y, c, None, (*via_y, c), True)]
            stages.append([
                (copy(a, 0, id_x, c, None, me),
                 [copy(a, 2, id_x, c, 0, (*via_y, c)), copy(a, 4, id_x, c, None, sibling)]),
                (copy(a, 1, id_y, c, None, me),
                 [copy(a, 3, id_y, c, 1, (*via_x, c)), copy(a, 5, id_y, c, None, sibling)]),
                (copy(a, 2, id_diagonal, c, 0, me), [copy(a, 6, id_diagonal, c, 0, sibling)]),
                (copy(a, 3, id_diagonal, c, 1, me), [copy(a, 7, id_diagonal, c, 1, sibling)]),
            ])
            last += [copy(a, 4, id_x, 1 - c, None, me), copy(a, 5, id_y, 1 - c, None, me),
                     copy(a, 6, id_diagonal, 1 - c, 0, me), copy(a, 7, id_diagonal, 1 - c, 1, me)]
        for b in range(nw):
            for k, (chip, index) in enumerate(((via_x, id_x), (via_y, id_y), (diagonal, id_diagonal))):
                first.append(whole_copy(b, k, 2 * x + y, (*chip, c)))
                last.append(whole_copy(b, k, index, me))
        return first, stages, last

    def start(*refs):
        for cp in plan(*refs)[0]:
            cp.start()

    def pass_on(stages, which):
        for stage in which:
            for per_shard in stages:
                lands, onward = per_shard[stage]
                lands.wait_recv()
                for cp in onward:
                    cp.start()

    def middle(*refs):
        pass_on(plan(*refs)[1], (0, 1))

    def rest(*refs):
        first, stages, last = plan(*refs)
        pass_on(stages, (2, 3))
        for cp in last:
            cp.wait_recv()
        for cp in first + [cp for per_shard in stages for _, onward in per_shard for cp in onward]:
            cp.wait_send()

    out_shape = [jax.ShapeDtypeStruct((NCHIP,) + s.shape, s.dtype) for s in list(shards) + list(whole)]
    return _Exchange(list(shards) + list(whole), out_shape, slots * n + 3 * nw, start, middle=middle, rest=rest)


def _with_own(gathered, own):
    x, y, _ = _place()
    return lax.dynamic_update_index_in_dim(gathered, own, 2 * x + y, axis=0)


def _simple_exchange(inputs, out_shape, copies_of):
    def start(*refs):
        for cp in copies_of(*refs):
            cp.start()

    def finish(*refs):
        for cp in copies_of(*refs):
            cp.wait()

    return _Exchange(list(inputs), out_shape, len(out_shape) * 3, start, finish)


def _grad_pair_sum(a, b, *, by_columns, name):
    m, n = a.shape[1], b.shape[1]
    rows, cols = (m, n // NCHIP) if by_columns else (m // NCHIP, n)
    half = rows // 2

    def body(a_ref, b_ref, out_ref, mine_ref, send_ref, recv_ref, send_sems, recv_sems):
        i = pl.program_id(0)
        x, y, c = _place()

        def share(t):
            return pltpu.make_async_remote_copy(
                src_ref=send_ref.at[t], dst_ref=recv_ref.at[t], send_sem=send_sems.at[t], recv_sem=recv_sems.at[t],
                device_id=(x, y, 1 - c), device_id_type=MESH)

        @pl.when(i < NCHIP)
        def _():
            block = _dot(a_ref[...], b_ref[...], 0, 0).astype(BF16)
            south = c == 0
            mine_ref[i % 2] = jnp.where(south, block[:half], block[half:])
            send_ref[i] = jnp.where(south, block[half:], block[:half])
            share(i).start()

        @pl.when(i > 0)
        def _():
            share(i - 1).wait_recv()
            out_ref[...] = (mine_ref[(i - 1) % 2].astype(F32) + recv_ref[i - 1].astype(F32)).astype(BF16)

        @pl.when(i == NCHIP)
        def _():
            for t in range(NCHIP):
                share(t).wait_send()

    def block_of(i):
        return jnp.minimum(i, NCHIP - 1)

    if by_columns:
        in_specs = [pl.BlockSpec((S, m), lambda i: (0, 0)), pl.BlockSpec((S, cols), lambda i: (0, block_of(i)))]
    else:
        in_specs = [pl.BlockSpec((S, rows), lambda i: (0, block_of(i))), pl.BlockSpec((S, n), lambda i: (0, 0))]
    return pl.pallas_call(
        body, name=name, grid=(NCHIP + 1,), in_specs=in_specs,
        out_specs=pl.BlockSpec((None, half, cols), lambda i: (jnp.maximum(i - 1, 0), 0, 0)),
        out_shape=jax.ShapeDtypeStruct((NCHIP, half, cols), BF16),
        scratch_shapes=[pltpu.VMEM((2, half, cols), BF16), pltpu.VMEM((NCHIP, half, cols), BF16),
                        pltpu.VMEM((NCHIP, half, cols), BF16), pltpu.SemaphoreType.DMA((NCHIP,)),
                        pltpu.SemaphoreType.DMA((NCHIP,))],
        compiler_params=_cparams(),
    )(a, b)


def _chip_exchange(parts):
    def copies_of(src, outs, send_sems, recv_sems):
        x, y, c = _place()
        return [pltpu.make_async_remote_copy(
            src_ref=src[a].at[2 * chip[0] + chip[1]], dst_ref=outs[a].at[k], send_sem=send_sems.at[3 * a + k],
            recv_sem=recv_sems.at[3 * a + k], device_id=(*chip, c), device_id_type=MESH)
            for a in range(len(src)) for k, chip in enumerate(_other_chips(x, y))]

    return _simple_exchange(parts, [jax.ShapeDtypeStruct((NCHIP - 1,) + p.shape[1:], p.dtype) for p in parts], copies_of)


def _chip_sum(parts, received, exchange=None):
    n = len(parts)
    steps = 4
    ex_in, ex_in_specs, ex_out_specs, ex_out_shape, ex_scratch = _hosted(exchange)
    n_ex = len(ex_in)

    def body(*refs):
        mine, theirs = refs[2 * n + n_ex:3 * n + n_ex], refs[3 * n + n_ex:4 * n + n_ex]
        ex_refs = refs[2 * n:2 * n + n_ex] + refs[4 * n + n_ex:len(refs) - n - 2]
        tiles, send_sems, recv_sems = refs[len(refs) - n - 2:len(refs) - 2], refs[-2], refs[-1]
        step = pl.program_id(0)
        if exchange is not None:
            @pl.when(step == 0)
            def _():
                exchange.start(*exchange.split(ex_refs))

        x, y, c = _place()

        def share(a, i):
            rows = tiles[a].shape[1]
            return pltpu.make_async_remote_copy(
                src_ref=tiles[a].at[i], dst_ref=theirs[a].at[pl.ds(pl.multiple_of(i * rows, rows), rows)],
                send_sem=send_sems.at[a * steps + i], recv_sem=recv_sems.at[a * steps + i], device_id=(x, y, 1 - c),
                device_id_type=MESH)

        chip = 2 * x + y
        for a in range(n):
            p, r = refs[a], refs[n + a]
            own = jnp.where(chip == 0, p[0], jnp.where(chip == 1, p[1], jnp.where(chip == 2, p[2], p[3])))
            total = ((own.astype(F32) + r[0].astype(F32)) + r[1].astype(F32)) + r[2].astype(F32)
            mine[a][...] = total
            tiles[a][step] = total
            share(a, step).start()

        @pl.when(step == steps - 1)
        def _():
            if exchange is not None:
                exchange.finish(*exchange.split(ex_refs))
            for a in range(n):
                for i in range(steps):
                    share(a, i).wait()

    def specs(arrs):
        return [pl.BlockSpec((g.shape[0], g.shape[1] // steps, g.shape[2]), lambda i: (0, i, 0)) for g in arrs]

    out_specs = [pl.BlockSpec((g.shape[1] // steps, g.shape[2]), lambda i: (i, 0)) for g in parts]
    halves = [jax.ShapeDtypeStruct(g.shape[1:], F32) for g in parts]
    res = pl.pallas_call(
        body, name="grads_chip_sum", grid=(steps,), in_specs=specs(parts) + specs(received) + ex_in_specs,
        out_specs=out_specs + [HBM] * n + ex_out_specs, out_shape=halves * 2 + ex_out_shape,
        scratch_shapes=ex_scratch + [pltpu.VMEM((steps, g.shape[1] // steps, g.shape[2]), F32) for g in parts]
        + [pltpu.SemaphoreType.DMA((n * steps,))] * 2, compiler_params=_cparams(),
    )(*parts, *received, *ex_in)
    return res[:n], res[n:2 * n], res[2 * n:]


def _adamw_math(w, g, m, v):
    nm = ADAM_B1 * m + (1.0 - ADAM_B1) * g
    nv = ADAM_B2 * v + (1.0 - ADAM_B2) * jnp.square(g)
    m_hat = nm / (1.0 - ADAM_B1 ** ADAM_STEP)
    v_hat = nv / (1.0 - ADAM_B2 ** ADAM_STEP)
    return -ADAM_LR * (m_hat / (jnp.sqrt(v_hat) + ADAM_EPS) + ADAM_WD * w), nm, nv


def _adamw_big(ws, g_mine, g_theirs, ms, vs, exchange=None):
    n = len(ws)
    steps = 8
    ex_in, ex_in_specs, ex_out_specs, ex_out_shape, ex_scratch = _hosted(exchange)

    def body(*refs):
        ex_refs = refs[5 * n:5 * n + len(ex_in)] + refs[9 * n + len(ex_in):]
        outs = refs[5 * n + len(ex_in):9 * n + len(ex_in)]
        if exchange is not None:
            @pl.when(pl.program_id(0) == 0)
            def _():
                exchange.start(*exchange.split(ex_refs))

        own_half = (pl.program_id(0) // (steps // 2)) == lax.axis_index("c")
        for a in range(n):
            g = jnp.where(own_half, refs[n + a][...], refs[2 * n + a][...])
            d, nm, nv = _adamw_math(refs[a][...], g, refs[3 * n + a][...], refs[4 * n + a][...])
            outs[a][...] = g
            outs[n + a][...] = d
            outs[2 * n + a][...] = nm
            outs[3 * n + a][...] = nv

        if exchange is not None:
            @pl.when(pl.program_id(0) == steps - 1)
            def _():
                exchange.finish(*exchange.split(ex_refs))

    specs = [pl.BlockSpec((w.shape[0] // steps, w.shape[1]), lambda i: (i, 0)) for w in ws]
    half_specs = [pl.BlockSpec((g.shape[0] // (steps // 2), g.shape[1]), lambda i: (i % (steps // 2), 0)) for g in g_mine]
    shapes = [jax.ShapeDtypeStruct(w.shape, F32) for w in ws]
    res = pl.pallas_call(
        body, name="adamw_big", grid=(steps,), in_specs=specs + half_specs * 2 + specs * 2 + ex_in_specs,
        out_specs=specs * 4 + ex_out_specs, out_shape=shapes * 4 + ex_out_shape, scratch_shapes=ex_scratch,
        compiler_params=_cparams(),
    )(*ws, *g_mine, *g_theirs, *ms, *vs, *ex_in)
    return res[:n], res[n:2 * n], res[2 * n:3 * n], res[3 * n:4 * n], res[4 * n:]


def _adamw_in(w, g_mine, g_theirs, m, v):
    half = D // 2

    def body(w_ref, gm_ref, gt_ref, m_ref, v_ref, g_out, d_out, nm_out, nv_out, g_ref):
        south = lax.axis_index("c") == 0
        g_ref[0:half, :] = jnp.where(south, gm_ref[...], gt_ref[...])
        g_ref[half:D, :] = jnp.where(south, gt_ref[...], gm_ref[...])
        g = g_ref[0:CW, :]
        d, nm, nv = _adamw_math(w_ref[...], g, m_ref[...], v_ref[...])
        g_out[...] = g
        d_out[...] = d
        nm_out[...] = nm
        nv_out[...] = nv

    spec = pl.BlockSpec((CW, LANES), lambda i: (0, i))
    half_spec = pl.BlockSpec((half, LANES), lambda i: (0, i))
    return pl.pallas_call(
        body, name="adamw_in", grid=(D // LANES,), in_specs=[spec, half_spec, half_spec, spec, spec], out_specs=[spec] * 4,
        out_shape=[jax.ShapeDtypeStruct((CW, D), F32)] * 4, scratch_shapes=[pltpu.VMEM((D, LANES), F32)],
        compiler_params=_cparams(),
    )(w, g_mine, g_theirs, m, v)


NORM_NAMES = ("pre_mix_norm", "post_mix_norm", "pre_mlp_norm", "post_mlp_norm")
SMALL_NAMES = NORM_NAMES + ("gdn_conv_w", "fox_f_bias", "gdn_dt_bias", "gdn_a_log", "fox_out_norm", "gdn_out_norm")
CONV_COLS = 3 * DGDN // NCHIP


def _small_gather(d_norms, d_conv, sums, d_fox_norm, d_gdn_norm, loss_row):
    n_arrays = 6
    n_remote = n_arrays * (NDEV - 1)

    def copies_of(src, outs, send_sems, recv_sems):
        x, y, c = _place()
        me = 4 * x + 2 * y + c

        def from_me(chip_index):
            cols = pl.ds(pl.multiple_of(chip_index * CONV_COLS, LANES), CONV_COLS)
            return [src[0], src[1].at[:, cols], src[2], src[3], src[4], src[5]]

        local = [pltpu.make_async_copy(s, outs[a].at[me], send_sems.at[n_remote + a]) for a, s in enumerate(from_me(2 * x + y))]
        remote = []
        for k in range(1, NDEV):
            px, py, pc = x ^ ((k >> 2) & 1), y ^ ((k >> 1) & 1), c ^ (k & 1)
            remote += [pltpu.make_async_remote_copy(
                src_ref=s, dst_ref=outs[a].at[me], send_sem=send_sems.at[n_arrays * (k - 1) + a],
                recv_sem=recv_sems.at[n_arrays * (k - 1) + a], device_id=(px, py, pc), device_id_type=MESH)
                for a, s in enumerate(from_me(2 * px + py))]
        return local + remote

    def start(*refs):
        for cp in copies_of(*refs):
            cp.start()

    def finish(*refs):
        for cp in copies_of(*refs):
            cp.wait()

    shapes = [(4, D), (CONV_K, CONV_COLS), (8, LANES), (1, LANES), (1, LANES), (1, LANES)]
    return _Exchange([d_norms, d_conv, sums, d_fox_norm, d_gdn_norm, loss_row],
                     [jax.ShapeDtypeStruct((NDEV,) + s, F32) for s in shapes], n_remote + n_arrays, start, finish)


def _small_adamw(gathered, ws, ms, vs):
    n = len(SMALL_NAMES)
    ng = len(gathered)

    def body(*refs):
        def total(buf):
            acc = buf[0]
            for i in range(1, NDEV):
                acc = acc + buf[i]
            return acc

        t_norms, t_conv, t_sums, t_fn, t_gn, t_loss = [total(r) for r in refs[:ng]]
        w_refs, m_refs, v_refs = refs[ng:ng + n], refs[ng + n:ng + 2 * n], refs[ng + 2 * n:ng + 3 * n]
        outs = refs[ng + 3 * n:]
        outs[4 * n][...] = t_loss
        grads = [t_norms[i:i + 1, :] for i in range(4)] + [
            t_conv, t_sums[0:1, 0:NFH], t_sums[1:2, 0:NGH], t_sums[2:3, 0:NGH], t_fn[:, 0:FHD], t_gn]
        for a in range(n):
            d, nm, nv = _adamw_math(w_refs[a][...], grads[a], m_refs[a][...], v_refs[a][...])
            outs[a][...] = grads[a]
            outs[n + a][...] = d
            outs[2 * n + a][...] = nm
            outs[3 * n + a][...] = nv

    def whole(arr):
        return pl.BlockSpec(arr.shape, lambda i: (0,) * arr.ndim)

    res = pl.pallas_call(
        body, name="small_adamw", grid=(1,), in_specs=[whole(t) for t in gathered] + [whole(w) for w in ws] * 3,
        out_specs=[whole(w) for w in ws] * 4 + [pl.BlockSpec((1, LANES), lambda i: (0, 0))],
        out_shape=[jax.ShapeDtypeStruct(w.shape, F32) for w in ws] * 4 + [jax.ShapeDtypeStruct((1, LANES), F32)],
        compiler_params=_cparams(),
    )(*gathered, *ws, *ms, *vs)
    return res[:n], res[n:2 * n], res[2 * n:3 * n], res[3 * n:4 * n], res[4 * n]


CW = DPROJ // NCHIP
PROJ_RUNS = tuple((part * DFOX + hp * LANES, part * DFOX + (hp + 1) * LANES, (3 * hp + part) * LANES)
                  for hp in range(NPAIR) for part in range(3)) + (
    (1536, 1544, BLK_SMALL * LANES), (1544, 3080, BLK_GDN * LANES), (3080, 3088, BLK_SMALL * LANES + 8),
    (3088, 3600, BLK_GZ * LANES))


def _proj_pieces():
    pieces = []
    for lo, hi, at in PROJ_RUNS:
        while lo < hi:
            j = lo // CW
            end = min(hi, (j + 1) * CW)
            pieces.append((j, lo - j * CW, at, end - lo))
            at, lo = at + end - lo, end
    return pieces


RT = 256


def _to_padded_rows(gathered):
    def body(src_ref, out_ref, blocks_ref, rows_ref):
        blocks_ref[...] = src_ref[...].astype(F32)
        rows_ref[...] = jnp.zeros_like(rows_ref)
        for j, start, at, n in _proj_pieces():
            rows_ref[at:at + n, :] = blocks_ref[j, start:start + n, :]
        out_ref[...] = rows_ref[...].astype(out_ref.dtype)

    return pl.pallas_call(
        body, name="proj_rows_in", grid=(D // RT,), in_specs=[pl.BlockSpec((NCHIP, D, RT), lambda i: (0, 0, i))],
        out_specs=pl.BlockSpec((DPROJ_PAD, RT), lambda i: (0, i)), out_shape=jax.ShapeDtypeStruct((DPROJ_PAD, D), gathered.dtype),
        scratch_shapes=[pltpu.VMEM((NCHIP, D, RT), F32), pltpu.VMEM((DPROJ_PAD, RT), F32)], compiler_params=_cparams(),
    )(gathered)


def _from_padded_rows_pair_sum(w):
    steps = D // RT
    half = D // 2

    def body(src_ref, out_ref, rows_ref, blocks_ref, mine_ref, send_ref, recv_ref, send_sems, recv_sems):
        i = pl.program_id(0)
        x, y, c = _place()

        def share(t):
            return pltpu.make_async_remote_copy(
                src_ref=send_ref.at[t], dst_ref=recv_ref.at[t], send_sem=send_sems.at[t], recv_sem=recv_sems.at[t],
                device_id=(x, y, 1 - c), device_id_type=MESH)

        def rows_of(core):
            return blocks_ref[:, pl.ds(pl.multiple_of(core * half, half), half), :]

        @pl.when(i < steps)
        def _():
            rows_ref[...] = src_ref[...].astype(F32)
            blocks_ref[...] = jnp.zeros_like(blocks_ref)
            for j, start, at, n in _proj_pieces():
                blocks_ref[j, start:start + n, :] = rows_ref[at:at + n, :]
            mine_ref[i % 2] = rows_of(c)
            send_ref[i] = rows_of(1 - c).astype(BF16)
            share(i).start()

        @pl.when(i > 0)
        def _():
            share(i - 1).wait_recv()
            out_ref[...] = (mine_ref[(i - 1) % 2] + recv_ref[i - 1].astype(F32)).astype(BF16)

        @pl.when(i == steps)
        def _():
            for t in range(steps):
                share(t).wait_send()

    tile = (NCHIP, half, RT)
    return pl.pallas_call(
        body, name="proj_rows_out_pair_sum", grid=(steps + 1,),
        in_specs=[pl.BlockSpec((DPROJ_PAD, RT), lambda i: (0, jnp.minimum(i, steps - 1)))],
        out_specs=pl.BlockSpec(tile, lambda i: (0, 0, jnp.maximum(i - 1, 0))),
        out_shape=jax.ShapeDtypeStruct((NCHIP, half, D), BF16),
        scratch_shapes=[pltpu.VMEM((DPROJ_PAD, RT), F32), pltpu.VMEM((NCHIP, D, RT), F32), pltpu.VMEM((2,) + tile, F32),
                        pltpu.VMEM((steps,) + tile, BF16), pltpu.VMEM((steps,) + tile, BF16),
                        pltpu.SemaphoreType.DMA((steps,)), pltpu.SemaphoreType.DMA((steps,))],
        compiler_params=_cparams(),
    )(w)


def _local_step(x, target, first_weights, late_weights, grad_matmul, reduce_late, reduce_in, pre_mix_norm, fox_f_bias,
                fox_out_norm, gdn_a_log, gdn_dt_bias, gdn_out_norm, post_mix_norm, pre_mlp_norm, post_mlp_norm):
    bias_vec = jnp.zeros((1, LANES), F32).at[0, 0:NFH].set(fox_f_bias).at[0, LANE_G:LANE_G + NGH].set(gdn_dt_bias)
    alog_vec = jnp.zeros((1, LANES), F32).at[0, LANE_G:LANE_G + NGH].set(gdn_a_log)
    w2 = jnp.concatenate([fox_out_norm, fox_out_norm], axis=1)

    h, first = _pre_norm(x, pre_mix_norm, exchange=first_weights[0])
    win_p, conv_w = first_weights[1](first)
    proj = _matmul(h, win_p, tb=True, tm=2048, tn=768, tk=1024, name="mm_proj", exchange=late_weights[0])
    proj, late_a = proj if late_weights[0] is not None else (proj, [])
    gates = _gates(proj, bias_vec, alog_vec)
    mix, fox_o, lse, late_b = _fox_fwd(proj, gates, w2, exchange=late_weights[1])
    qkv = _gdn_pre(proj, conv_w)
    (u, w, qd, kd, a_intra, gl, t_inv), _ = _gdn_prep(qkv, gates)
    wout, wup3 = late_weights[3](late_a, late_b)
    mix, gdn_raw, states = _gdn_scan(u, w, qd, kd, a_intra, gl, proj, gdn_out_norm, mix)
    def post_mix(acc, xv, w_post, w_pre_mlp):
        x1v = xv + acc * _rms_scale(acc) * w_post
        return acc, x1v, x1v * _rms_scale(x1v) * w_pre_mlp

    mixed, x1, h2 = _matmul(mix, wout, tm=512, tn=D, tk=1024, out_dtypes=(F32, F32, BF16), name="mm_out",
                            extra=(x, post_mix_norm, pre_mlp_norm), epilogue=post_mix)

    def relu2(acc):
        r = jnp.maximum(acc, 0.0)
        return r, r * r

    up_act = _matmul(h2, wup3, b3=True, tm=1024, tn=1024, tk=1024, out_dtypes=(BF16, BF16), epilogue=relu2,
                     name="mm_up", exchange=late_weights[2])
    (up_relu, act), late_c = up_act if late_weights[2] is not None else (up_act, [])
    wdown = late_weights[4](late_c)
    def loss_head(acc, x1v, tv, w):
        err = x1v + acc * _rms_scale(acc) * w - tv
        dx2v = err * (1.0 / D)
        dyv, dwt = _rms_bwd(acc, w, dx2v)
        part = 0.5 * jnp.sum(jnp.mean(err * err, axis=-1, keepdims=True), axis=0, keepdims=True)
        return dx2v, dyv, jnp.sum(dwt, axis=0, keepdims=True), jnp.broadcast_to(part, (1, D))

    dx2, dy, d_post_mlp, loss_wide = _matmul(
        act, wdown, tm=512, tn=D, tk=DFF, out_dtypes=(F32, BF16, F32, F32), extra=(x1, target, post_mlp_norm),
        epilogue=loss_head, n_sums=2, name="mm_down")
    loss_row = loss_wide[:, :LANES]

    dwdown = grad_matmul(act, dy, by_columns=False, name="mm_dwdown")

    def relu2_bwd(acc, r):
        return (acc * 2.0 * r.astype(F32),)

    dup = _matmul(dy, wdown, tb=True, tm=1024, tn=1024, tk=1024, out_dtypes=(BF16,), extra=(up_relu,), epilogue=relu2_bwd,
                  name="mm_dact")
    dwup3 = grad_matmul(h2, dup, by_columns=True, name="mm_dwup")

    def mid_bwd(acc, x1v, dx2v, mixedv, w_pre_mlp, w_post):
        dxa, dwm = _rms_bwd(x1v, w_pre_mlp, acc)
        dx1v = dx2v + dxa
        dm, dwp = _rms_bwd(mixedv, w_post, dx1v)
        return dx1v, dm, jnp.sum(dwm, axis=0, keepdims=True), jnp.sum(dwp, axis=0, keepdims=True)

    dx1, dmixed, d_pre_mlp, d_post_mix = _matmul(
        dup, wup3, tb=True, b3=True, tm=512, tn=D, tk=DFF, out_dtypes=(F32, BF16, F32, F32),
        extra=(x1, dx2, mixed, pre_mlp_norm, post_mix_norm), epilogue=mid_bwd, n_sums=2, name="mm_dh2")
    dwout = grad_matmul(mix, dmixed, by_columns=False, name="mm_dwout")
    dmix = _matmul(dmixed, wout, tb=True, tm=2048, tk=1024, name="mm_dmix")

    dfox, delta, d_fox_norm = _fox_norm_bwd(fox_o, dmix, w2)
    dproj, dcum_fox, reduced_a = _fox_bwd(proj, dfox, gates, lse, delta, exchange=reduce_late[0](dwout, dwup3, dwdown))
    (dproj, du, dw, dqd, dkd, da, dgl, d_gdn_norm), reduced_b = _gdn_scan_bwd(
        dmix, gdn_raw, proj, gdn_out_norm, u, w, qd, kd, a_intra, gl, states, dproj, exchange=reduce_late[1]())
    dqkv, dgates_gdn, reduced_c = _gdn_prep_bwd(qkv, gates, t_inv, du, dw, dqd, dkd, da, dgl, exchange=reduce_late[2]())
    reduced_late = (reduced_a, reduced_b, reduced_c)
    dproj, d_conv = _gdn_pre_bwd(proj, conv_w, dqkv, dproj)
    dproj, sums = _gates_bwd(proj, bias_vec, alog_vec, dgates_gdn, dcum_fox, dproj)

    dwin_p = _matmul(dproj, h, ta=True, tm=1280, tn=1024, tk=2048, out_dtypes=(BF16,), name="mm_dwin")
    exchange_in = reduce_in(dwin_p)
    dh = _matmul(dproj, win_p, tm=1024, tk=DPROJ_PAD, name="mm_dh", exchange=exchange_in)
    dh, reduced_in = dh if exchange_in is not None else (dh, [])
    grad_x, d_pre_mix = _pre_norm_bwd(dh, x, pre_mix_norm, dx1)

    d_norms = jnp.concatenate([d_pre_mix, d_post_mix, d_pre_mlp, d_post_mlp], axis=0)
    return grad_x, (d_norms, d_conv, sums, d_fox_norm, d_gdn_norm, loss_row), reduced_late, reduced_in


def kernel(x, pre_mix_norm, w_in, fox_f_bias, fox_out_norm, gdn_conv_w, gdn_a_log, gdn_dt_bias, gdn_out_norm, w_out, post_mix_norm, pre_mlp_norm, w_up, w_down, post_mlp_norm, loss_target, m_pre_mix_norm, m_w_in, m_fox_f_bias, m_fox_out_norm, m_gdn_conv_w, m_gdn_a_log, m_gdn_dt_bias, m_gdn_out_norm, m_w_out, m_post_mix_norm, m_pre_mlp_norm, m_w_up, m_w_down, m_post_mlp_norm, v_pre_mix_norm, v_w_in, v_fox_f_bias, v_fox_out_norm, v_gdn_conv_w, v_gdn_a_log, v_gdn_dt_bias, v_gdn_out_norm, v_w_out, v_post_mix_norm, v_pre_mlp_norm, v_w_up, v_w_down, v_post_mlp_norm):
    weights = dict(pre_mix_norm=pre_mix_norm, w_in=w_in, fox_f_bias=fox_f_bias, fox_out_norm=fox_out_norm, gdn_conv_w=gdn_conv_w,
                   gdn_a_log=gdn_a_log, gdn_dt_bias=gdn_dt_bias, gdn_out_norm=gdn_out_norm, w_out=w_out, post_mix_norm=post_mix_norm,
                   pre_mlp_norm=pre_mlp_norm, w_up=w_up, w_down=w_down, post_mlp_norm=post_mlp_norm)
    m_in = dict(pre_mix_norm=m_pre_mix_norm, w_in=m_w_in, fox_f_bias=m_fox_f_bias, fox_out_norm=m_fox_out_norm, gdn_conv_w=m_gdn_conv_w,
                gdn_a_log=m_gdn_a_log, gdn_dt_bias=m_gdn_dt_bias, gdn_out_norm=m_gdn_out_norm, w_out=m_w_out, post_mix_norm=m_post_mix_norm,
                pre_mlp_norm=m_pre_mlp_norm, w_up=m_w_up, w_down=m_w_down, post_mlp_norm=m_post_mlp_norm)
    v_in = dict(pre_mix_norm=v_pre_mix_norm, w_in=v_w_in, fox_f_bias=v_fox_f_bias, fox_out_norm=v_fox_out_norm, gdn_conv_w=v_gdn_conv_w,
                gdn_a_log=v_gdn_a_log, gdn_dt_bias=v_gdn_dt_bias, gdn_out_norm=v_gdn_out_norm, w_out=v_w_out, post_mix_norm=v_post_mix_norm,
                pre_mlp_norm=v_pre_mlp_norm, w_up=v_w_up, w_down=v_w_down, post_mlp_norm=v_post_mlp_norm)
    order_w = ("pre_mix_norm", "w_in", "fox_f_bias", "fox_out_norm", "gdn_conv_w", "gdn_a_log", "gdn_dt_bias", "gdn_out_norm", "w_out",
               "post_mix_norm", "pre_mlp_norm", "w_up", "w_down", "post_mlp_norm")
    big = ("w_in", "w_out", "w_up", "w_down")

    def row(v):
        return v if v.ndim == 2 else v.reshape(1, -1)

    win_shard = jnp.pad(w_in.T.astype(BF16), ((0, D - CW), (0, 0)))

    def resolve_first(gathered):
        win_g, conv_g = gathered
        return (_to_padded_rows(_with_own(win_g, win_shard)),
                _with_own(conv_g, gdn_conv_w).transpose(1, 0, 2).reshape(CONV_K, 3 * DGDN))

    late_shards = [weights[n].astype(BF16) for n in big[1:]]

    gathered_down = []

    def resolve_out_up(gathered_out, gathered_mlp):
        gathered_down.append(gathered_mlp[1])
        return _with_own(gathered_out[0], late_shards[0]).reshape(D, D), _with_own(gathered_mlp[0], late_shards[1])

    def resolve_down(_):
        return _with_own(gathered_down[0], late_shards[2]).reshape(DFF, D)

    pair_sums = {}

    def late_chip_exchange(dwout, dwup3, dwdown):
        pair_sums.update(w_out=dwout, w_up=dwup3, w_down=dwdown)
        return _chip_exchange([pair_sums["w_up"], pair_sums["w_down"]])

    def reduce_in(dwin_p):
        pair_sums["w_in"] = _from_padded_rows_pair_sum(dwin_p)
        return _chip_exchange([pair_sums["w_in"]])

    grad_x, small, received_late, received_in = _local_step(
        x[0], loss_target[0], (_allgather_exchange([win_shard], whole=[gdn_conv_w]), resolve_first),
        (_allgather_exchange(late_shards[:1]), _allgather_exchange(late_shards[1:]), None, resolve_out_up, resolve_down),
        _grad_pair_sum, (late_chip_exchange, lambda: None, lambda: _chip_exchange([pair_sums["w_out"]])),
        reduce_in, row(pre_mix_norm), fox_f_bias, row(fox_out_norm), gdn_a_log, gdn_dt_bias,
        row(gdn_out_norm), row(post_mix_norm), row(pre_mlp_norm), row(post_mlp_norm))
    received_mlp, _, received_out = received_late

    g_mine, g_theirs, small_gathered = _chip_sum(
        [pair_sums[n] for n in big], list(received_in[:1]) + list(received_out[:1]) + list(received_mlp[:2]),
        exchange=_small_gather(*small))

    g_big, d_big, nm_big, nv_big, _ = _adamw_big(
        [weights[n] for n in big[1:]], g_mine[1:], g_theirs[1:], [m_in[n] for n in big[1:]], [v_in[n] for n in big[1:]])
    in_t = _adamw_in(w_in.T, g_mine[0], g_theirs[0], m_w_in.T, v_w_in.T)
    g_small, d_small, nm_small, nv_small, loss_total = _small_adamw(
        small_gathered, [row(weights[n]) for n in SMALL_NAMES], [row(m_in[n]) for n in SMALL_NAMES],
        [row(v_in[n]) for n in SMALL_NAMES])

    grads, delta, new_m, new_v = {}, {}, {}, {}
    grads["w_in"], delta["w_in"], new_m["w_in"], new_v["w_in"] = [t.T for t in in_t]
    for i, n in enumerate(big[1:]):
        grads[n], delta[n], new_m[n], new_v[n] = g_big[i], d_big[i], nm_big[i], nv_big[i]
    for i, n in enumerate(SMALL_NAMES):
        shape = weights[n].shape
        grads[n], delta[n], new_m[n], new_v[n] = (g_small[i].reshape(shape), d_small[i].reshape(shape),
                                                  nm_small[i].reshape(shape), nv_small[i].reshape(shape))
    return (loss_total[0, 0], grad_x[None], *[grads[n] for n in order_w], *[delta[n] for n in order_w], *[new_m[n] for n in order_w],
            *[new_v[n] for n in order_w])
```

```python
import jax
import jax.numpy as jnp
from jax import lax
from jax.experimental import pallas as pl
from jax.experimental.pallas import tpu as pltpu

F32 = jnp.float32
BF16 = jnp.bfloat16
MESH = pl.DeviceIdType.MESH

S = 2048
D = 1024
NFH, FHD = 8, 64
NPAIR = NFH // 2
NGH, GHD = 4, 128
DFOX = NFH * FHD
DGDN = NGH * GHD
CHUNK = 64
NCH = S // CHUNK
CONV_K = 4
DFF = 4 * D
EPS = 1e-6
DPROJ = 3600
LANES = 128
DPROJ_PAD = 3840
BLK_GDN = 12
BLK_GZ = 24
BLK_SMALL = 28
NCHIP = 4
NDEV = 8
VMEM_LIMIT = 56 * 1024 * 1024

ADAM_LR = 0.001
ADAM_B1 = 0.9
ADAM_B2 = 0.999
ADAM_EPS = 1e-08
ADAM_WD = 0.01
ADAM_STEP = 10


def _cparams(**kw):
    return pltpu.CompilerParams(vmem_limit_bytes=VMEM_LIMIT, **kw)


def _dn(ca, cb):
    return (((ca,), (cb,)), ((), ()))


def _dot(a, b, ca=1, cb=0):
    return lax.dot_general(a.astype(BF16), b.astype(BF16), _dn(ca, cb), preferred_element_type=F32)


def _hdot(a, b, ca=1, cb=0):
    return lax.dot_general(a.astype(F32), b.astype(F32), _dn(ca, cb), precision=lax.Precision.HIGHEST,
                           preferred_element_type=F32)


def _dot3(a, b, ca=1, cb=0):
    a_hi, b_hi = a.astype(BF16), b.astype(BF16)
    a_lo, b_lo = (a - a_hi.astype(F32)).astype(BF16), (b - b_hi.astype(F32)).astype(BF16)
    dn = _dn(ca, cb)
    return (lax.dot_general(a_hi, b_hi, dn, preferred_element_type=F32)
            + (lax.dot_general(a_hi, b_lo, dn, preferred_element_type=F32)
               + lax.dot_general(a_lo, b_hi, dn, preferred_element_type=F32)))


@jax.custom_vjp
def _mm_nn(a, b):
    return _dot(a, b, 1, 0)


def _mm_nn_fwd(a, b):
    return _dot(a, b, 1, 0), (a, b)


def _mm_nn_bwd(res, g):
    a, b = res
    return _dot(g, b, 1, 1), _dot(a, g, 0, 0)


_mm_nn.defvjp(_mm_nn_fwd, _mm_nn_bwd)


@jax.custom_vjp
def _mm_nt(a, b):
    return _dot(a, b, 1, 1)


def _mm_nt_fwd(a, b):
    return _dot(a, b, 1, 1), (a, b)


def _mm_nt_bwd(res, g):
    a, b = res
    return _dot(g, b, 1, 0), _dot(g, a, 0, 0)


_mm_nt.defvjp(_mm_nt_fwd, _mm_nt_bwd)


@jax.custom_vjp
def _saved_inverse(m, t_inv):
    del m
    return t_inv


def _saved_inverse_fwd(m, t_inv):
    del m
    return t_inv, t_inv


def _saved_inverse_bwd(t_inv, g):
    return -_dot3(_dot3(t_inv, g, 0, 0), t_inv, 1, 1), jnp.zeros_like(t_inv)


_saved_inverse.defvjp(_saved_inverse_fwd, _saved_inverse_bwd)


def _sigmoid(z):
    return 1.0 / (1.0 + jnp.exp(-z))


def _softplus(z):
    return jnp.maximum(z, 0.0) + jnp.log(1.0 + jnp.exp(-jnp.abs(z)))


def _silu(z):
    return z * _sigmoid(z)


def _rms_scale(x):
    return lax.rsqrt(jnp.mean(x * x, axis=-1, keepdims=True) + EPS)


def _rms_bwd(x, w, g):
    r = _rms_scale(x)
    gw = g * w
    dx = r * gw - x * (r * r * r) * jnp.mean(gw * x, axis=-1, keepdims=True)
    return dx, g * x * r


def _matmul(a, b, *, name, ta=False, tb=False, tm=512, tn=512, tk=512, out_dtypes=(F32,), b3=False, o3=False,
            extra=(), epilogue=None, exchange=None, n_sums=0):
    m, k = (a.shape[1], a.shape[0]) if ta else a.shape
    if b3:
        n = b.shape[1] if tb else b.shape[0] * b.shape[2]
        kb = b.shape[0] * b.shape[2] if tb else b.shape[1]
    else:
        n, kb = (b.shape[0], b.shape[1]) if tb else (b.shape[1], b.shape[0])
    assert kb == k, (name, kb, k)
    tm, tn, tk = min(tm, m), min(tn, n), min(tk, k)
    assert m % tm == 0 and n % tn == 0 and k % tk == 0, (name, m, n, k, tm, tn, tk)
    nk = k // tk
    whole_k_blocks = b3 and tb and not ta and nk == 1 and b.shape[0] > 1
    n_extra = len(extra)
    n_out = len(out_dtypes)
    grid = (m // tm, n // tn, nk)
    ex_in, ex_in_specs, ex_out_specs, ex_out_shape, ex_scratch = _hosted(exchange)

    def body(*refs):
        a_ref, b_ref = refs[0], refs[1]
        extra_refs = refs[2:2 + n_extra]
        first_out = 2 + n_extra + len(ex_in)
        out_refs = refs[first_out:first_out + n_out]
        ex_refs = refs[2 + n_extra:first_out] + refs[first_out + n_out:first_out + n_out + len(ex_out_shape)] + refs[-2:]
        step = [pl.program_id(d) for d in range(3)]

        if exchange is not None:
            @pl.when((step[0] == 0) & (step[1] == 0) & (step[2] == 0))
            def _():
                exchange.start(*exchange.split(ex_refs))

        def finish(acc):
            outs = (acc,) if epilogue is None else epilogue(acc, *[r[...] for r in extra_refs])
            for o_ref, val in zip(out_refs[:n_out - n_sums], outs):
                o_ref[...] = val.astype(o_ref.dtype)
            for o_ref, val in zip(out_refs[n_out - n_sums:], outs[n_out - n_sums:]):
                @pl.when(step[0] == 0)
                def _(o_ref=o_ref, val=val):
                    o_ref[...] = val

                @pl.when(step[0] > 0)
                def _(o_ref=o_ref, val=val):
                    o_ref[...] += val

        if whole_k_blocks:
            width = b.shape[2]
            part = _dot(a_ref[:, 0:width], b_ref[0], 1, 1)
            for blk in range(1, b.shape[0]):
                part = part + _dot(a_ref[:, blk * width:(blk + 1) * width], b_ref[blk], 1, 1)
        else:
            part = _dot(a_ref[...], b_ref[...], 0 if ta else 1, 1 if tb else 0)
        if nk == 1:
            finish(part)
        else:
            acc_ref = refs[first_out + n_out + len(ex_out_shape)]

            @pl.when(step[2] == 0)
            def _():
                acc_ref[...] = part

            @pl.when(step[2] > 0)
            def _():
                acc_ref[...] += part

            @pl.when(step[2] == nk - 1)
            def _():
                finish(acc_ref[...])

        if exchange is not None:
            flat = (step[0] * grid[1] + step[1]) * nk + step[2]
            total = grid[0] * grid[1] * nk

            @pl.when(flat == total // 2)
            def _():
                exchange.middle(*exchange.split(ex_refs))

            @pl.when(flat == total - 1)
            def _():
                exchange.rest(*exchange.split(ex_refs))

    a_spec = pl.BlockSpec((tk, tm), lambda i, j, kk: (kk, i)) if ta else pl.BlockSpec((tm, tk), lambda i, j, kk: (i, kk))
    if whole_k_blocks:
        b_spec = pl.BlockSpec((b.shape[0], tn, b.shape[2]), lambda i, j, kk: (0, j, 0))
    elif b3 and tb:
        assert b.shape[2] == tk
        b_spec = pl.BlockSpec((None, tn, tk), lambda i, j, kk: (kk, j, 0))
    elif b3:
        assert b.shape[2] == tn
        b_spec = pl.BlockSpec((None, tk, tn), lambda i, j, kk: (j, kk, 0))
    elif tb:
        b_spec = pl.BlockSpec((tn, tk), lambda i, j, kk: (j, kk))
    else:
        b_spec = pl.BlockSpec((tk, tn), lambda i, j, kk: (kk, j))
    tile = pl.BlockSpec((tm, tn), lambda i, j, kk: (i, j))
    out_specs = [tile] * n_out
    out_shape = [jax.ShapeDtypeStruct((m, n), dt) for dt in out_dtypes]
    if o3:
        out_specs[0] = pl.BlockSpec((None, tm, tn), lambda i, j, kk: (j, i, 0))
        out_shape[0] = jax.ShapeDtypeStruct((n // tn, m, tn), out_dtypes[0])
    assert n_sums == 0 or tn == n
    for r in range(n_out - n_sums, n_out):
        out_specs[r] = pl.BlockSpec((1, tn), lambda i, j, kk: (0, 0))
        out_shape[r] = jax.ShapeDtypeStruct((1, n), out_dtypes[r])
    res = pl.pallas_call(
        body, name=name, grid=grid,
        in_specs=[a_spec, b_spec] + [tile if e.shape[0] == m else pl.BlockSpec((1, tn), lambda i, j, kk: (0, j)) for e in extra]
        + ex_in_specs, out_specs=out_specs + ex_out_specs,
        out_shape=out_shape + ex_out_shape,
        scratch_shapes=([pltpu.VMEM((tm, tn), F32)] if nk > 1 else []) + ex_scratch,
        compiler_params=_cparams(),
    )(a, b, *extra, *ex_in)
    if exchange is not None:
        return (res[0] if n_out == 1 else res[:n_out]), res[n_out:]
    return res[0] if n_out == 1 else res


TR = 256


def _row_spec(cols):
    return pl.BlockSpec((TR, cols), lambda i: (i, 0))


def _vec_spec(cols):
    return pl.BlockSpec((1, cols), lambda i: (0, 0))


def _pre_norm(x, w, exchange=None):
    ex_in, ex_in_specs, ex_out_specs, ex_out_shape, ex_scratch = _hosted(exchange)

    def body(*refs):
        x_ref, w_ref, h_ref = refs[0], refs[1], refs[2 + len(ex_in)]
        ex_refs = refs[2:2 + len(ex_in)] + refs[3 + len(ex_in):]
        if exchange is not None:
            @pl.when(pl.program_id(0) == 0)
            def _():
                exchange.start(*exchange.split(ex_refs))

        xv = x_ref[...]
        h_ref[...] = (xv * _rms_scale(xv) * w_ref[...]).astype(BF16)

        if exchange is not None:
            @pl.when(pl.program_id(0) == S // TR - 1)
            def _():
                exchange.finish(*exchange.split(ex_refs))

    res = pl.pallas_call(
        body, name="pre_norm", grid=(S // TR,), in_specs=[_row_spec(D), _vec_spec(D)] + ex_in_specs,
        out_specs=[_row_spec(D)] + ex_out_specs, out_shape=[jax.ShapeDtypeStruct((S, D), BF16)] + ex_out_shape,
        scratch_shapes=ex_scratch, compiler_params=_cparams(),
    )(x, w, *ex_in)
    return res[0], res[1:]


def _pre_norm_bwd(dh, x, w, dx1):
    def body(dh_ref, x_ref, w_ref, dx1_ref, dx_ref, dw_ref):
        i = pl.program_id(0)
        dxa, dwt = _rms_bwd(x_ref[...], w_ref[...], dh_ref[...])
        dx_ref[...] = dx1_ref[...] + dxa

        @pl.when(i == 0)
        def _():
            dw_ref[...] = jnp.zeros_like(dw_ref)

        dw_ref[...] += jnp.sum(dwt, axis=0, keepdims=True)

    return pl.pallas_call(
        body, name="pre_norm_bwd", grid=(S // TR,),
        in_specs=[_row_spec(D), _row_spec(D), _vec_spec(D), _row_spec(D)], out_specs=[_row_spec(D), _vec_spec(D)],
        out_shape=[jax.ShapeDtypeStruct((S, D), F32), jax.ShapeDtypeStruct((1, D), F32)], compiler_params=_cparams(),
    )(dh, x, w, dx1)


BQ = 512
NQ = S // BQ
LANE_BETA, LANE_G = 8, 12


def _gate_lanes(shape):
    lane = lax.broadcasted_iota(jnp.int32, shape, 1)
    return lane < LANE_BETA, (lane >= LANE_BETA) & (lane < LANE_G), (lane >= LANE_G) & (lane < LANE_G + NGH)


def _gates(proj, bias_vec, alog_vec):
    def body(s_ref, b_ref, a_ref, o_ref, carry_ref):
        i = pl.program_id(0)

        @pl.when(i == 0)
        def _():
            carry_ref[...] = jnp.zeros_like(carry_ref)

        z = s_ref[...] + b_ref[...]
        tail = jnp.log(1.0 + jnp.exp(-jnp.abs(z)))
        sp = jnp.maximum(z, 0.0) + tail
        lf = jnp.minimum(z, 0.0) - tail
        r = lax.broadcasted_iota(jnp.int32, (BQ, BQ), 0)
        c = lax.broadcasted_iota(jnp.int32, (BQ, BQ), 1)
        tri = (c <= r).astype(F32)
        cum = _hdot(tri, lf) + carry_ref[...]
        carry_ref[...] = cum[BQ - 1:BQ, :]
        is_fox, is_beta, is_g = _gate_lanes(z.shape)
        o_ref[...] = jnp.where(is_fox, cum, jnp.where(is_beta, _sigmoid(z), jnp.where(is_g, -jnp.exp(a_ref[...]) * sp, 0.0)))

    return pl.pallas_call(
        body, name="gates", grid=(NQ,),
        in_specs=[pl.BlockSpec((BQ, LANES), lambda i: (i, BLK_SMALL)), _vec_spec(LANES), _vec_spec(LANES)],
        out_specs=pl.BlockSpec((BQ, LANES), lambda i: (i, 0)), out_shape=jax.ShapeDtypeStruct((S, LANES), F32),
        scratch_shapes=[pltpu.VMEM((1, LANES), F32)], compiler_params=_cparams(),
    )(proj, bias_vec, alog_vec)


def _gates_bwd(proj, bias_vec, alog_vec, dgates_gdn, dcum_fox, dproj):
    def body(s_ref, b_ref, a_ref, dg_ref, dc_ref, dproj_in, dproj_ref, red_ref, carry_ref):
        del dproj_in
        i = pl.program_id(0)

        @pl.when(i == 0)
        def _():
            carry_ref[...] = jnp.zeros_like(carry_ref)
            red_ref[...] = jnp.zeros_like(red_ref)

        z = s_ref[...] + b_ref[...]
        dg = dg_ref[...] + dc_ref[...]
        r = lax.broadcasted_iota(jnp.int32, (BQ, BQ), 0)
        c = lax.broadcasted_iota(jnp.int32, (BQ, BQ), 1)
        upper = (c >= r).astype(F32)
        dlf = _hdot(upper, dg) + carry_ref[...]
        carry_ref[...] = dlf[0:1, :]
        sig = _sigmoid(z)
        g_scale = -jnp.exp(a_ref[...])
        is_fox, is_beta, is_g = _gate_lanes(z.shape)
        ds = jnp.where(is_fox, dlf * (1.0 - sig), jnp.where(is_beta, dg * sig * (1.0 - sig), jnp.where(is_g, dg * g_scale * sig, 0.0)))
        dproj_ref[:, 0:LANES] = ds.astype(BF16)
        dproj_ref[:, LANES:2 * LANES] = jnp.zeros((BQ, LANES), BF16)
        dalog = jnp.where(is_g, dg * g_scale * _softplus(z), 0.0)
        sums = jnp.sum(ds, axis=0, keepdims=True)
        red_ref[0:1, :] += jnp.where(is_fox[0:1], sums, 0.0)
        red_ref[1:2, :] += pltpu.roll(jnp.where(is_g[0:1], sums, 0.0), LANES - LANE_G, 1)
        red_ref[2:3, :] += pltpu.roll(jnp.sum(dalog, axis=0, keepdims=True), LANES - LANE_G, 1)

    blk = pl.BlockSpec((BQ, LANES), lambda i: (NQ - 1 - i, 0))
    return pl.pallas_call(
        body, name="gates_bwd", grid=(NQ,),
        in_specs=[pl.BlockSpec((BQ, LANES), lambda i: (NQ - 1 - i, BLK_SMALL)), _vec_spec(LANES), _vec_spec(LANES), blk, blk,
                  pl.BlockSpec(memory_space=pl.ANY)],
        out_specs=[pl.BlockSpec((BQ, 2 * LANES), lambda i: (NQ - 1 - i, BLK_SMALL // 2)), pl.BlockSpec((8, LANES), lambda i: (0, 0))],
        out_shape=[jax.ShapeDtypeStruct((S, DPROJ_PAD), BF16), jax.ShapeDtypeStruct((8, LANES), F32)],
        input_output_aliases={5: 0},
        scratch_shapes=[pltpu.VMEM((1, LANES), F32)], compiler_params=_cparams(),
    )(proj, bias_vec, alog_vec, dgates_gdn, dcum_fox, dproj)


FOX_SCALE = FHD ** -0.5
FOX_PAIRS = 2
FOX_PAIRS_BWD = 2


def _head_mask(e):
    lane = lax.broadcasted_iota(jnp.int32, (1, LANES), 1)
    return (lane >= e * FHD) & (lane < (e + 1) * FHD)


def _lane_col(vals, index):
    lane = lax.broadcasted_iota(jnp.int32, vals.shape, 1)
    return jnp.sum(jnp.where(lane == index, vals, 0.0), axis=1, keepdims=True)


def _sublane_row(vals, index):
    row = lax.broadcasted_iota(jnp.int32, vals.shape, 0)
    return jnp.sum(jnp.where(row == index, vals, 0.0), axis=0, keepdims=True)


def _pair_cols(c0, c1):
    lane = lax.broadcasted_iota(jnp.int32, (c0.shape[0], 2), 1)
    return jnp.where(lane == 0, c0, c1)


def _split3(x):
    hi = x.astype(BF16).astype(F32)
    rest = x - hi
    mid = rest.astype(BF16).astype(F32)
    return hi, mid, (rest - mid).astype(BF16).astype(F32)


def _fox_operand(vals, e, cum, is_query):
    lane = lax.broadcasted_iota(jnp.int32, (1, LANES), 1)
    base = (1 - e) * FHD
    parts = _split3(cum)
    own = jnp.where(_head_mask(e), vals * FOX_SCALE if is_query else vals, 0.0)
    cum_at, ones_at = (base, base + 3) if is_query else (base + 3, base)
    sign = 1.0 if is_query else -1.0
    out = own + jnp.where((lane >= ones_at) & (lane < ones_at + 3), 1.0, 0.0)
    for i, part in enumerate(parts):
        out = out + jnp.where(lane == cum_at + i, sign * part, 0.0)
    return out.astype(BF16)


def _causal_block():
    return lax.broadcasted_iota(jnp.int32, (BQ, BQ), 1) <= lax.broadcasted_iota(jnp.int32, (BQ, BQ), 0)


def _head_rms(o, masks):
    o2 = o * o
    r = [lax.rsqrt(jnp.sum(jnp.where(mk, o2, 0.0), axis=1, keepdims=True) * (1.0 / FHD) + EPS) for mk in masks]
    return jnp.where(masks[0], r[0], r[1])


def _hosted(exchange):
    if exchange is None:
        return [], [], [], [], []
    return (exchange.inputs, [HBM] * len(exchange.inputs), [HBM] * len(exchange.out_shape), exchange.out_shape,
            exchange.sem_shapes())


def _fox_fwd(proj, gates, w2, exchange=None):
    ex_in, ex_in_specs, ex_out_specs, ex_out_shape, ex_scratch = _hosted(exchange)

    n_in = 3 * FOX_PAIRS + 2
    heads = [(pp, e) for pp in range(FOX_PAIRS) for e in range(2)]

    def body(*refs):
        qkv_refs, g_ref, w_ref = refs[:3 * FOX_PAIRS], refs[3 * FOX_PAIRS], refs[3 * FOX_PAIRS + 1]
        mix_ref, o_ref, lse_ref = refs[n_in + len(ex_in):n_in + 3 + len(ex_in)]
        ka_ref, vb_ref = refs[n_in + 3 + len(ex_in) + len(ex_out_shape):n_in + 5 + len(ex_in) + len(ex_out_shape)]
        ex_refs = refs[n_in:n_in + len(ex_in)] + refs[n_in + 3 + len(ex_in):n_in + 3 + len(ex_in) + len(ex_out_shape)] + refs[-2:]
        grp, qi = pl.program_id(0), pl.program_id(1)

        def head_index(pp, e):
            return 2 * (FOX_PAIRS * grp + pp) + e

        if exchange is not None:
            @pl.when((grp == 0) & (qi == 0))
            def _():
                exchange.start(*exchange.split(ex_refs))

        @pl.when(qi == 0)
        def _():
            gt = g_ref[...]
            for pp in range(FOX_PAIRS):
                kv = qkv_refs[3 * pp + 1][...]
                for e in range(2):
                    ka_ref[2 * pp + e] = _fox_operand(kv, e, _lane_col(gt, head_index(pp, e)), False)
                vb_ref[pp] = qkv_refs[3 * pp + 2][...].astype(BF16)

        masks = [_head_mask(0), _head_mask(1)]
        gt = g_ref[pl.ds(pl.multiple_of(qi * BQ, BQ), BQ), :]
        qs = [_fox_operand(qkv_refs[3 * pp][...], e, _lane_col(gt, head_index(pp, e)), True) for pp, e in heads]
        n = range(len(heads))

        def block(kj, carry, diagonal):
            rows = pl.ds(pl.multiple_of(kj * BQ, BQ), BQ)
            s = [_dot(qs[i], ka_ref[i, rows, :], 1, 1) for i in n]
            if diagonal:
                s = [jnp.where(_causal_block(), s[i], -jnp.inf) for i in n]
            m_new = [jnp.maximum(carry[i][0], jnp.max(s[i], axis=-1, keepdims=True)) for i in n]
            p = [jnp.exp(s[i] - m_new[i]) for i in n]
            alpha = [jnp.exp(carry[i][0] - m_new[i]) for i in n]
            l_new = [alpha[i] * carry[i][1] + jnp.sum(p[i], axis=-1, keepdims=True) for i in n]
            pv = [_dot(p[i], vb_ref[heads[i][0], rows, :]) for i in n]
            return tuple((m_new[i], l_new[i], alpha[i] * carry[i][2] + pv[i]) for i in n)

        one = (jnp.full((BQ, 1), -jnp.inf, F32), jnp.zeros((BQ, 1), F32), jnp.zeros((BQ, LANES), F32))
        below = lax.fori_loop(0, qi, lambda kj, carry: block(kj, carry, False), (one,) * len(heads))
        done = block(qi, below, True)
        for pp in range(FOX_PAIRS):
            (m0, l0, a0), (m1, l1, a1) = done[2 * pp], done[2 * pp + 1]
            o = jnp.where(masks[0], a0 / l0, a1 / l1)
            cols = slice(pp * LANES, (pp + 1) * LANES)
            o_ref[:, cols] = o
            mix_ref[:, cols] = (o * _head_rms(o, masks) * w_ref[...]).astype(BF16)
            lse_ref[pp] = _pair_cols(m0 + jnp.log(l0), m1 + jnp.log(l1))

        if exchange is not None:
            @pl.when((grp == NPAIR // FOX_PAIRS // 2) & (qi == 0))
            def _():
                exchange.middle(*exchange.split(ex_refs))

            @pl.when((grp == NPAIR // FOX_PAIRS - 1) & (qi == NQ - 1))
            def _():
                exchange.rest(*exchange.split(ex_refs))

    qkv_specs = []
    for pp in range(FOX_PAIRS):
        qkv_specs.append(pl.BlockSpec((BQ, LANES), lambda g, i, pp=pp: (i, 3 * (FOX_PAIRS * g + pp))))
        qkv_specs.append(pl.BlockSpec((S, LANES), lambda g, i, pp=pp: (0, 3 * (FOX_PAIRS * g + pp) + 1)))
        qkv_specs.append(pl.BlockSpec((S, LANES), lambda g, i, pp=pp: (0, 3 * (FOX_PAIRS * g + pp) + 2)))
    blk = pl.BlockSpec((BQ, FOX_PAIRS * LANES), lambda g, i: (i, g))
    res = pl.pallas_call(
        body, name="fox_fwd", grid=(NPAIR // FOX_PAIRS, NQ),
        in_specs=qkv_specs + [pl.BlockSpec((S, LANES), lambda g, i: (0, 0)), pl.BlockSpec((1, LANES), lambda g, i: (0, 0))]
        + ex_in_specs,
        out_specs=[blk, blk, pl.BlockSpec((FOX_PAIRS, BQ, 2), lambda g, i: (g, i, 0))] + ex_out_specs,
        out_shape=[jax.ShapeDtypeStruct((S, D), BF16), jax.ShapeDtypeStruct((S, DFOX), F32),
                   jax.ShapeDtypeStruct((NPAIR, S, 2), F32)] + ex_out_shape,
        scratch_shapes=[pltpu.VMEM((2 * FOX_PAIRS, S, LANES), BF16), pltpu.VMEM((FOX_PAIRS, S, LANES), BF16)] + ex_scratch,
        compiler_params=_cparams(),
    )(*([proj] * (3 * FOX_PAIRS)), gates, w2, *ex_in)
    return res[0], res[1], res[2], res[3:]


def _fox_norm_bwd(o, dmix, w2):
    def body(o_ref, g_ref, w_ref, do_ref, dl_ref, dw_ref):
        hp, qi = pl.program_id(0), pl.program_id(1)
        masks = [_head_mask(0), _head_mask(1)]
        ov = o_ref[...]
        g = g_ref[...]
        r = _head_rms(ov, masks)
        gw = g * w_ref[...]
        gwo = gw * ov
        mean = [jnp.sum(jnp.where(mk, gwo, 0.0), axis=1, keepdims=True) * (1.0 / FHD) for mk in masks]
        do = r * gw - ov * (r * r * r) * jnp.where(masks[0], mean[0], mean[1])
        do_ref[...] = do.astype(BF16)
        doo = do * ov
        dl_ref[...] = _pair_cols(*[jnp.sum(jnp.where(mk, doo, 0.0), axis=1, keepdims=True) for mk in masks])

        @pl.when((hp == 0) & (qi == 0))
        def _():
            dw_ref[...] = jnp.zeros_like(dw_ref)

        dw_ref[...] += jnp.sum(g * ov * r, axis=0, keepdims=True)

        @pl.when((hp == NPAIR - 1) & (qi == NQ - 1))
        def _():
            dw = dw_ref[...]
            dw_ref[...] = dw + pltpu.roll(dw, FHD, 1)

    blk = pl.BlockSpec((BQ, LANES), lambda hp, i: (i, hp))
    vec = pl.BlockSpec((1, LANES), lambda hp, i: (0, 0))
    return pl.pallas_call(
        body, name="fox_norm_bwd", grid=(NPAIR, NQ), in_specs=[blk, blk, vec],
        out_specs=[blk, pl.BlockSpec((None, BQ, 2), lambda hp, i: (hp, i, 0)), vec],
        out_shape=[jax.ShapeDtypeStruct((S, DFOX), BF16), jax.ShapeDtypeStruct((NPAIR, S, 2), F32),
                   jax.ShapeDtypeStruct((1, LANES), F32)],
        compiler_params=_cparams(),
    )(o, dmix, w2)


def _fox_bwd(proj, do, gates, lse, delta, exchange=None):
    ex_in, ex_in_specs, ex_out_specs, ex_out_shape, ex_scratch = _hosted(exchange)

    pg = FOX_PAIRS_BWD
    n_in = 3 * pg + 4
    heads = [(pp, e) for pp in range(pg) for e in range(2)]

    def body(*refs):
        qkv_refs = refs[:3 * pg]
        do_ref, g_ref, lse_ref, dl_ref = refs[3 * pg:n_in]
        dproj_ref, dc_ref = refs[n_in + len(ex_in):n_in + 2 + len(ex_in)]
        qa_ref, dq_ref = refs[n_in + 2 + len(ex_in) + len(ex_out_shape):n_in + 4 + len(ex_in) + len(ex_out_shape)]
        ex_refs = refs[n_in:n_in + len(ex_in)] + refs[n_in + 2 + len(ex_in):n_in + 2 + len(ex_in) + len(ex_out_shape)] + refs[-2:]
        grp, kj = pl.program_id(0), pl.program_id(1)

        def head_index(pp, e):
            return 2 * (pg * grp + pp) + e

        if exchange is not None:
            @pl.when((grp == 0) & (kj == 0))
            def _():
                exchange.start(*exchange.split(ex_refs))

        @pl.when(kj == 0)
        def _():
            gt = g_ref[...]
            for pp in range(pg):
                qv = qkv_refs[3 * pp][...]
                for e in range(2):
                    qa_ref[2 * pp + e] = _fox_operand(qv, e, _lane_col(gt, head_index(pp, e)), True)
            dq_ref[...] = jnp.zeros_like(dq_ref)

        @pl.when((grp == 0) & (kj == 0))
        def _():
            dc_ref[...] = jnp.zeros_like(dc_ref)

        masks = [_head_mask(0), _head_mask(1)]
        krows = pl.ds(pl.multiple_of(kj * BQ, BQ), BQ)
        gk = g_ref[krows, :]
        kas = [_fox_operand(qkv_refs[3 * pp + 1][...], e, _lane_col(gk, head_index(pp, e)), False) for pp, e in heads]
        vbs = [qkv_refs[3 * pp + 2][...].astype(BF16) for pp in range(pg)]
        lane = lax.broadcasted_iota(jnp.int32, (BQ, LANES), 1)
        n = range(len(heads))

        def block(qi, carry, diagonal):
            dks, dvs, css = carry
            rows = pl.ds(pl.multiple_of(qi * BQ, BQ), BQ)
            qa = [qa_ref[i, rows, :] for i in n]
            s = [_dot(qa[i], kas[i], 1, 1) for i in n]
            if diagonal:
                s = [jnp.where(_causal_block(), s[i], -jnp.inf) for i in n]
            dov = [do_ref[rows, pp * LANES:(pp + 1) * LANES] for pp in range(pg)]
            doe = [jnp.where(masks[e], dov[pp], jnp.zeros_like(dov[pp])) for pp, e in heads]
            lse2 = [lse_ref[pp, rows, :] for pp in range(pg)]
            dl2 = [dl_ref[pp, rows, :] for pp in range(pg)]
            p = [jnp.exp(s[i] - _lane_col(lse2[heads[i][0]], heads[i][1])) for i in n]
            dp = [_dot(doe[i], vbs[heads[i][0]], 1, 1) for i in n]
            ds = [p[i] * (dp[i] - _lane_col(dl2[heads[i][0]], heads[i][1])) for i in n]
            dv_part = [_dot(p[i], doe[i], 0, 0) for i in n]
            dk_part = [_dot(ds[i], jnp.where(masks[heads[i][1]], qa[i], jnp.zeros_like(qa[i])), 0, 0) for i in n]
            dq_part = [jnp.where(masks[heads[i][1]], _dot(ds[i], kas[i]), 0.0) for i in n]
            css = tuple(css[i] + jnp.sum(ds[i], axis=0, keepdims=True) for i in n)
            dc = jnp.zeros((BQ, LANES), F32)
            for i in n:
                dc = dc + jnp.where(lane == head_index(*heads[i]), jnp.sum(ds[i], axis=1, keepdims=True), 0.0)
            for pp in range(pg):
                dq_ref[pp, rows, :] += (dq_part[2 * pp] + dq_part[2 * pp + 1]) * FOX_SCALE
            dc_ref[rows, :] += dc
            dks = tuple(dks[pp] + dk_part[2 * pp] + dk_part[2 * pp + 1] for pp in range(pg))
            dvs = tuple(dvs[pp] + dv_part[2 * pp] + dv_part[2 * pp + 1] for pp in range(pg))
            return dks, dvs, css

        zero = jnp.zeros((BQ, LANES), F32)
        first = block(kj, ((zero,) * pg, (zero,) * pg, (jnp.zeros((1, BQ), F32),) * len(heads)), True)
        dks, dvs, css = lax.fori_loop(kj + 1, NQ, lambda qi, carry: block(qi, carry, False), first)
        r = lax.broadcasted_iota(jnp.int32, (BQ, BQ), 0)
        c = lax.broadcasted_iota(jnp.int32, (BQ, BQ), 1)
        dcol = jnp.zeros((BQ, LANES), F32)
        for i in n:
            col = jnp.sum(jnp.where(r == c, css[i], 0.0), axis=1, keepdims=True)
            dcol = dcol + jnp.where(lane == head_index(*heads[i]), col, 0.0)
        dc_ref[krows, :] -= dcol
        for pp in range(pg):
            base = 3 * pp * LANES
            dproj_ref[krows, base + LANES:base + 2 * LANES] = dks[pp].astype(BF16)
            dproj_ref[krows, base + 2 * LANES:base + 3 * LANES] = dvs[pp].astype(BF16)

        @pl.when(kj == NQ - 1)
        def _():
            for pp in range(pg):
                dproj_ref[:, 3 * pp * LANES:(3 * pp + 1) * LANES] = dq_ref[pp].astype(BF16)

        if exchange is not None:
            @pl.when((grp == NPAIR // pg // 2) & (kj == 0))
            def _():
                exchange.middle(*exchange.split(ex_refs))

            @pl.when((grp == NPAIR // pg - 1) & (kj == NQ - 1))
            def _():
                exchange.rest(*exchange.split(ex_refs))

    qkv_specs = []
    for pp in range(pg):
        qkv_specs.append(pl.BlockSpec((S, LANES), lambda g, j, pp=pp: (0, 3 * (pg * g + pp))))
        qkv_specs.append(pl.BlockSpec((BQ, LANES), lambda g, j, pp=pp: (j, 3 * (pg * g + pp) + 1)))
        qkv_specs.append(pl.BlockSpec((BQ, LANES), lambda g, j, pp=pp: (j, 3 * (pg * g + pp) + 2)))
    pair = pl.BlockSpec((pg, S, 2), lambda g, j: (g, 0, 0))
    res = pl.pallas_call(
        body, name="fox_bwd", grid=(NPAIR // pg, NQ),
        in_specs=qkv_specs + [pl.BlockSpec((S, pg * LANES), lambda g, j: (0, g)), pl.BlockSpec((S, LANES), lambda g, j: (0, 0)),
                              pair, pair] + ex_in_specs,
        out_specs=[pl.BlockSpec((S, 3 * pg * LANES), lambda g, j: (0, g)), pl.BlockSpec((S, LANES), lambda g, j: (0, 0))]
        + ex_out_specs,
        out_shape=[jax.ShapeDtypeStruct((S, DPROJ_PAD), BF16), jax.ShapeDtypeStruct((S, LANES), F32)] + ex_out_shape,
        scratch_shapes=[pltpu.VMEM((2 * pg, S, LANES), BF16), pltpu.VMEM((pg, S, LANES), F32)] + ex_scratch,
        compiler_params=_cparams(),
    )(*([proj] * (3 * pg)), do, gates, lse, delta, *ex_in)
    return res[0], res[1], res[2:]


NQKV = 3 * NGH
GDN_QSCALE = GHD ** -0.5


def _shift_down(x, s):
    if s == 0:
        return x
    row = lax.broadcasted_iota(jnp.int32, x.shape, 0)
    return jnp.where(row >= s, pltpu.roll(x, s, 0), 0.0)


def _shift_up(x, s):
    if s == 0:
        return x
    n = x.shape[0]
    row = lax.broadcasted_iota(jnp.int32, x.shape, 0)
    return jnp.where(row < n - s, pltpu.roll(x, n - s, 0), 0.0)


def _conv_taps(xv):
    return [_shift_down(xv, CONV_K - 1 - j) for j in range(CONV_K)]


def _conv_pre(taps, wv):
    pre = taps[CONV_K - 1] * wv[CONV_K - 1:CONV_K, :]
    for j in range(CONV_K - 1):
        pre = pre + taps[j] * wv[j:j + 1, :]
    return pre


def _l2_factors(b):
    return b < 2 * NGH, jnp.where(b < NGH, GDN_QSCALE, 1.0)


def _gdn_pre(proj, conv_w):
    def body(x_ref, w_ref, o_ref):
        b = pl.program_id(0)
        c = _silu(_conv_pre(_conv_taps(x_ref[...]), w_ref[...]))
        normed, scale = _l2_factors(b)
        rs = lax.rsqrt(jnp.sum(c * c, axis=-1, keepdims=True) + EPS)
        o_ref[...] = c * jnp.where(normed, rs, 1.0) * scale

    return pl.pallas_call(
        body, name="gdn_pre", grid=(NQKV,),
        in_specs=[pl.BlockSpec((S, GHD), lambda b: (0, BLK_GDN + b)), pl.BlockSpec((CONV_K, GHD), lambda b: (0, b))],
        out_specs=pl.BlockSpec((S, GHD), lambda b: (0, b)),
        out_shape=jax.ShapeDtypeStruct((S, NQKV * GHD), F32), compiler_params=_cparams(),
    )(proj, conv_w)


def _gdn_pre_bwd(proj, conv_w, dqkv, dproj):
    def body(x_ref, w_ref, dy_ref, dproj_in, dx_ref, dw_ref):
        del dproj_in
        b = pl.program_id(0)
        taps = _conv_taps(x_ref[...])
        wv = w_ref[...]
        pre = _conv_pre(taps, wv)
        sig = _sigmoid(pre)
        c = pre * sig
        normed, scale = _l2_factors(b)
        g = dy_ref[...] * scale
        rs = lax.rsqrt(jnp.sum(c * c, axis=-1, keepdims=True) + EPS)
        dc_n = rs * g - c * (rs * rs * rs) * jnp.sum(g * c, axis=-1, keepdims=True)
        dc = jnp.where(normed, dc_n, g)
        dpre = dc * sig * (1.0 + pre * (1.0 - sig))
        dx = dpre * wv[CONV_K - 1:CONV_K, :]
        for j in range(CONV_K - 1):
            dx = dx + _shift_up(dpre, CONV_K - 1 - j) * wv[j:j + 1, :]
        dx_ref[...] = dx.astype(BF16)
        for j in range(CONV_K):
            dw_ref[j:j + 1, :] = jnp.sum(dpre * taps[j], axis=0, keepdims=True)

    return pl.pallas_call(
        body, name="gdn_pre_bwd", grid=(NQKV,),
        in_specs=[pl.BlockSpec((S, GHD), lambda b: (0, BLK_GDN + b)), pl.BlockSpec((CONV_K, GHD), lambda b: (0, b)),
                  pl.BlockSpec((None, S, GHD), lambda b: (b // NGH, 0, b % NGH)), pl.BlockSpec(memory_space=pl.ANY)],
        out_specs=[pl.BlockSpec((S, GHD), lambda b: (0, BLK_GDN + b)), pl.BlockSpec((CONV_K, GHD), lambda b: (0, b))],
        out_shape=[jax.ShapeDtypeStruct((S, DPROJ_PAD), BF16), jax.ShapeDtypeStruct((CONV_K, NQKV * GHD), F32)],
        input_output_aliases={3: 0}, compiler_params=_cparams(),
    )(proj, conv_w, dqkv, dproj)


CB = 16
NCB = NCH // CB


def _chunk_prep(qs, ks, vs, gcols, bcols, t_saved=None):
    n = range(len(qs))
    r = lax.broadcasted_iota(jnp.int32, (CHUNK, CHUNK), 0)
    c = lax.broadcasted_iota(jnp.int32, (CHUNK, CHUNK), 1)
    incl = c <= r
    eye = (r == c).astype(F32)
    grow = [jnp.sum(gcols[i] * eye, axis=0, keepdims=True) for i in n]
    gc_col = [jnp.sum(jnp.where(incl, grow[i], 0.0), axis=1, keepdims=True) for i in n]
    gc_row = [jnp.sum(jnp.where(r <= c, gcols[i], 0.0), axis=0, keepdims=True) for i in n]
    decay = [jnp.exp(jnp.where(incl, gc_col[i] - gc_row[i], -jnp.inf)) for i in n]
    kb = [ks[i] * bcols[i] for i in n]
    vb = [vs[i] * bcols[i] for i in n]
    kk = [_mm_nt(kb[i], ks[i]) for i in n]
    m = [jnp.where(c < r, kk[i] * decay[i], 0.0) for i in n]
    if t_saved is None:
        t_inv = [eye - m[i] for i in n]
        p = [_dot3(m[i], m[i]) for i in n]
        for step in range(5):
            t_inv = [t_inv[i] + _dot3(t_inv[i], p[i]) for i in n]
            if step < 4:
                p = [_dot3(p[i], p[i]) for i in n]
    else:
        t_inv = [_saved_inverse(m[i], t_saved[i]) for i in n]
    egc = [jnp.exp(gc_col[i]) for i in n]
    u = [_mm_nn(t_inv[i], vb[i]) for i in n]
    w = [_mm_nn(t_inv[i], kb[i] * egc[i]) for i in n]
    qk = [_mm_nt(qs[i], ks[i]) for i in n]
    gc_last = [gc_col[i][CHUNK - 1:CHUNK, :] for i in n]
    return [(u[i], w[i], qk[i] * decay[i], qs[i] * egc[i], ks[i] * jnp.exp(gc_last[i] - gc_col[i]), jnp.exp(gc_last[i]),
             t_inv[i]) for i in n]


def _prep_specs():
    rows = CB * CHUNK
    qs = pl.BlockSpec((rows, GHD), lambda i, h: (i, h))
    ks = pl.BlockSpec((rows, GHD), lambda i, h: (i, NGH + h))
    vs = pl.BlockSpec((rows, GHD), lambda i, h: (i, 2 * NGH + h))
    gs = pl.BlockSpec((rows, LANES), lambda i, h: (i, 0))
    a_s = pl.BlockSpec((None, rows, CHUNK), lambda i, h: (h, i, 0))
    gl_s = pl.BlockSpec((None, CB, 1, LANES), lambda i, h: (h, i, 0, 0))
    return qs, ks, vs, gs, a_s, gl_s


def _gdn_prep(qkv, gates, exchange=None):
    ex_in, ex_in_specs, ex_out_specs, ex_out_shape, ex_scratch = _hosted(exchange)

    def body(*refs):
        q_ref, k_ref, v_ref, g_ref = refs[:4]
        u_ref, w_ref, qd_ref, kd_ref, a_ref, gl_ref, t_ref = refs[4 + len(ex_in):11 + len(ex_in)]
        ex_refs = refs[4:4 + len(ex_in)] + refs[11 + len(ex_in):]
        h = pl.program_id(1)

        if exchange is not None:
            @pl.when((pl.program_id(0) == 0) & (h == 0))
            def _():
                exchange.start(*exchange.split(ex_refs))

        chunks = [pl.ds(cidx * CHUNK, CHUNK) for cidx in range(CB)]
        gts = [g_ref[rows, :] for rows in chunks]
        outs = _chunk_prep([q_ref[rows, :] for rows in chunks], [k_ref[rows, :] for rows in chunks],
                           [v_ref[rows, :] for rows in chunks], [_lane_col(gt, LANE_G + h) for gt in gts],
                           [_lane_col(gt, LANE_BETA + h) for gt in gts])
        for cidx, rows in enumerate(chunks):
            u, w, a, qd, kd, gl, t_inv = outs[cidx]
            u_ref[rows, :] = u
            w_ref[rows, :] = w
            qd_ref[rows, :] = qd
            kd_ref[rows, :] = kd
            a_ref[rows, :] = a
            t_ref[rows, :] = t_inv
            gl_ref[cidx] = jnp.broadcast_to(gl, (1, LANES))

        if exchange is not None:
            @pl.when((pl.program_id(0) == NCB // 2) & (h == 0))
            def _():
                exchange.middle(*exchange.split(ex_refs))

            @pl.when((pl.program_id(0) == NCB - 1) & (h == NGH - 1))
            def _():
                exchange.rest(*exchange.split(ex_refs))

    qs, ks, vs, gs, a_s, gl_s = _prep_specs()
    tok = jax.ShapeDtypeStruct((S, DGDN), F32)
    sq = jax.ShapeDtypeStruct((NGH, S, CHUNK), F32)
    res = pl.pallas_call(
        body, name="gdn_prep", grid=(NCB, NGH), in_specs=[qs, ks, vs, gs] + ex_in_specs,
        out_specs=[qs, qs, qs, qs, a_s, gl_s, a_s] + ex_out_specs,
        out_shape=[tok, tok, tok, tok, sq, jax.ShapeDtypeStruct((NGH, NCH, 1, LANES), F32), sq] + ex_out_shape,
        scratch_shapes=ex_scratch, compiler_params=_cparams(),
    )(qkv, qkv, qkv, gates, *ex_in)
    return res[:7], res[7:]


def _gdn_prep_bwd(qkv, gates, t_inv, du, dw, dqd, dkd, da, dgl, exchange=None):
    ex_in, ex_in_specs, ex_out_specs, ex_out_shape, ex_scratch = _hosted(exchange)

    def body(*refs):
        q_ref, k_ref, v_ref, g_ref, t_ref, du_ref, dw_ref, dqd_ref, dkd_ref, da_ref, dgl_ref = refs[:11]
        dqkv_ref, dg_ref = refs[11 + len(ex_in):13 + len(ex_in)]
        ex_refs = refs[11:11 + len(ex_in)] + refs[13 + len(ex_in):]
        h = pl.program_id(1)

        if exchange is not None:
            @pl.when((pl.program_id(0) == 0) & (h == 0))
            def _():
                exchange.start(*exchange.split(ex_refs))

        @pl.when(h == 0)
        def _():
            dg_ref[...] = jnp.zeros_like(dg_ref)

        lane = lax.broadcasted_iota(jnp.int32, (CHUNK, LANES), 1)
        chunks = [pl.ds(cidx * CHUNK, CHUNK) for cidx in range(CB)]
        gts = [g_ref[rows, :] for rows in chunks]
        t_saved = [t_ref[rows, :] for rows in chunks]
        _, vjp = jax.vjp(lambda *args: [o[:6] for o in _chunk_prep(*args, t_saved=t_saved)],
                         [q_ref[rows, :] for rows in chunks], [k_ref[rows, :] for rows in chunks],
                         [v_ref[rows, :] for rows in chunks], [_lane_col(gt, LANE_G + h) for gt in gts],
                         [_lane_col(gt, LANE_BETA + h) for gt in gts])
        dqs, dks, dvs, dgcs, dbcs = vjp([(du_ref[rows, :], dw_ref[rows, :], da_ref[rows, :], dqd_ref[rows, :],
                                          dkd_ref[rows, :], dgl_ref[cidx][:, 0:1]) for cidx, rows in enumerate(chunks)])
        for cidx, rows in enumerate(chunks):
            dq, dk, dv, dgc, dbc = dqs[cidx], dks[cidx], dvs[cidx], dgcs[cidx], dbcs[cidx]
            dqkv_ref[0, rows, :] = dq
            dqkv_ref[1, rows, :] = dk
            dqkv_ref[2, rows, :] = dv
            dg_ref[rows, :] += jnp.where(lane == LANE_G + h, dgc, 0.0) + jnp.where(lane == LANE_BETA + h, dbc, 0.0)

        if exchange is not None:
            @pl.when((pl.program_id(0) == NCB - 1) & (h == NGH - 1))
            def _():
                exchange.finish(*exchange.split(ex_refs))

    qs, ks, vs, gs, a_s, gl_s = _prep_specs()
    res = pl.pallas_call(
        body, name="gdn_prep_bwd", grid=(NCB, NGH), in_specs=[qs, ks, vs, gs, a_s, qs, qs, qs, qs, a_s, gl_s] + ex_in_specs,
        out_specs=[pl.BlockSpec((3, CB * CHUNK, GHD), lambda i, h: (0, i, h)), gs] + ex_out_specs,
        out_shape=[jax.ShapeDtypeStruct((3, S, DGDN), F32), jax.ShapeDtypeStruct((S, LANES), F32)] + ex_out_shape,
        scratch_shapes=ex_scratch, compiler_params=_cparams(),
    )(qkv, qkv, qkv, gates, t_inv, du, dw, dqd, dkd, da, dgl, *ex_in)
    return res[0], res[1], res[2:]


def _scan_specs(nh, parts, reverse):
    wide, rows, chunks = nh * GHD, S // parts, NCH // parts

    def part(p):
        return parts - 1 - p if reverse else p

    hs = pl.BlockSpec((rows, wide), lambda g, p: (part(p), g))
    a_s = pl.BlockSpec((nh, rows, CHUNK), lambda g, p: (g, part(p), 0))
    gl_s = pl.BlockSpec((nh, chunks, 1, LANES), lambda g, p: (g, part(p), 0, 0))
    st_s = pl.BlockSpec((nh, chunks, GHD, GHD), lambda g, p: (g, part(p), 0, 0))
    gz_s = pl.BlockSpec((rows, wide), lambda g, p: (part(p), BLK_GZ // nh + g))
    mix_s = pl.BlockSpec((rows, wide), lambda g, p: (part(p), NPAIR // nh + g))
    return hs, a_s, gl_s, st_s, gz_s, mix_s


def _head_cols(hh):
    return slice(hh * GHD, (hh + 1) * GHD)


SCAN_HEADS, SCAN_PARTS = 4, 2
SCAN_HEADS_BWD, SCAN_PARTS_BWD = 4, 4


def _gdn_scan(u, w, qd, kd, a, gl, proj, w_norm, mix):
    heads = range(SCAN_HEADS)

    def body(u_ref, w_ref, qd_ref, kd_ref, a_ref, gl_ref, z_ref, wn_ref, mix_in, mix_ref, o_ref, st_ref, carry_ref):
        del mix_in

        @pl.when(pl.program_id(1) == 0)
        def _():
            carry_ref[...] = jnp.zeros_like(carry_ref)

        def step(ci, states):
            rows = pl.ds(pl.multiple_of(ci * CHUNK, CHUNK), CHUNK)
            for hh in heads:
                st_ref[hh, ci] = states[hh]
            ws = [_dot(w_ref[rows, _head_cols(hh)], states[hh]) for hh in heads]
            qs = [_dot(qd_ref[rows, _head_cols(hh)], states[hh]) for hh in heads]
            vn = [u_ref[rows, _head_cols(hh)] - ws[hh] for hh in heads]
            av = [_dot(a_ref[hh, rows, :], vn[hh]) for hh in heads]
            kv = [_dot(kd_ref[rows, _head_cols(hh)], vn[hh], 0, 0) for hh in heads]
            for hh in heads:
                o_ref[rows, _head_cols(hh)] = qs[hh] + av[hh]
            return tuple(states[hh] * gl_ref[hh, ci] + kv[hh] for hh in heads)

        last = lax.fori_loop(0, NCH // SCAN_PARTS, step, tuple(carry_ref[hh] for hh in heads))
        for hh in heads:
            carry_ref[hh] = last[hh]
            ov = o_ref[:, _head_cols(hh)]
            mix_ref[:, _head_cols(hh)] = (ov * _rms_scale(ov) * wn_ref[...] * _silu(z_ref[:, _head_cols(hh)])).astype(BF16)

    hs, a_s, gl_s, st_s, gz_s, mix_s = _scan_specs(SCAN_HEADS, SCAN_PARTS, False)
    return pl.pallas_call(
        body, name="gdn_scan", grid=(NGH // SCAN_HEADS, SCAN_PARTS),
        in_specs=[hs, hs, hs, hs, a_s, gl_s, gz_s, pl.BlockSpec((1, GHD), lambda g, p: (0, 0)),
                  pl.BlockSpec(memory_space=pl.ANY)],
        out_specs=[mix_s, hs, st_s],
        out_shape=[jax.ShapeDtypeStruct((S, D), BF16), jax.ShapeDtypeStruct((S, DGDN), F32),
                   jax.ShapeDtypeStruct((NGH, NCH, GHD, GHD), F32)],
        input_output_aliases={8: 0}, scratch_shapes=[pltpu.VMEM((SCAN_HEADS, GHD, GHD), F32)], compiler_params=_cparams(),
    )(u, w, qd, kd, a, gl, proj, w_norm, mix)


def _gdn_scan_bwd(dmix, o, proj, w_norm, u, w, qd, kd, a, gl, states, dproj, exchange=None):
    ex_in, ex_in_specs, ex_out_specs, ex_out_shape, ex_scratch = _hosted(exchange)
    groups = NGH // SCAN_HEADS_BWD

    def body(*refs):
        dy_ref, o_ref, z_ref, wn_ref, u_ref, w_ref, qd_ref, kd_ref, a_ref, gl_ref, st_ref = refs[:11]
        dz_ref, du_ref, dw_ref, dqd_ref, dkd_ref, da_ref, dgl_ref, dwn_ref = refs[12 + len(ex_in):20 + len(ex_in)]
        do_ref, carry_ref = refs[20 + len(ex_in) + len(ex_out_shape):22 + len(ex_in) + len(ex_out_shape)]
        ex_refs = refs[12:12 + len(ex_in)] + refs[20 + len(ex_in):20 + len(ex_in) + len(ex_out_shape)] + refs[-2:]
        heads = range(SCAN_HEADS_BWD)
        chunks = NCH // SCAN_PARTS_BWD

        if exchange is not None:
            @pl.when((pl.program_id(0) == 0) & (pl.program_id(1) == 0))
            def _():
                exchange.start(*exchange.split(ex_refs))

        @pl.when((pl.program_id(0) == 0) & (pl.program_id(1) == 0))
        def _():
            dwn_ref[...] = jnp.zeros_like(dwn_ref)

        @pl.when(pl.program_id(1) == 0)
        def _():
            carry_ref[...] = jnp.zeros_like(carry_ref)

        wn = wn_ref[...]
        for hh in heads:
            c = _head_cols(hh)
            ov = o_ref[:, c]
            zv = z_ref[:, c]
            g = dy_ref[:, c]
            sig = _sigmoid(zv)
            dz_ref[:, c] = (g * (ov * _rms_scale(ov) * wn) * sig * (1.0 + zv * (1.0 - sig))).astype(BF16)
            do, dwt = _rms_bwd(ov, wn, g * zv * sig)
            do_ref[:, c] = do
            dwn_ref[...] += jnp.sum(dwt, axis=0, keepdims=True)

        def step(t, dstates):
            ci = chunks - 1 - t
            rows = pl.ds(pl.multiple_of(ci * CHUNK, CHUNK), CHUNK)
            cols = [_head_cols(hh) for hh in heads]
            state = [st_ref[hh, ci] for hh in heads]
            dov = [do_ref[rows, cols[hh]] for hh in heads]
            wv = [w_ref[rows, cols[hh]] for hh in heads]
            ws = [_dot(wv[hh], state[hh]) for hh in heads]
            adov = [_dot(a_ref[hh, rows, :], dov[hh], 0, 0) for hh in heads]
            kds = [_dot(kd_ref[rows, cols[hh]], dstates[hh]) for hh in heads]
            dqd = [_dot(dov[hh], state[hh], 1, 1) for hh in heads]
            qdo = [_dot(qd_ref[rows, cols[hh]], dov[hh], 0, 0) for hh in heads]
            vn = [u_ref[rows, cols[hh]] - ws[hh] for hh in heads]
            dvn = [adov[hh] + kds[hh] for hh in heads]
            da = [_dot(dov[hh], vn[hh], 1, 1) for hh in heads]
            dkd = [_dot(vn[hh], dstates[hh], 1, 1) for hh in heads]
            dwv = [_dot(dvn[hh], state[hh], 1, 1) for hh in heads]
            wdv = [_dot(wv[hh], dvn[hh], 0, 0) for hh in heads]
            for hh in heads:
                da_ref[hh, rows, :] = da[hh]
                dqd_ref[rows, cols[hh]] = dqd[hh]
                dkd_ref[rows, cols[hh]] = dkd[hh]
                dgl = jnp.sum(jnp.sum(dstates[hh] * state[hh], axis=1, keepdims=True), axis=0, keepdims=True)
                dgl_ref[hh, ci] = jnp.broadcast_to(dgl, (1, LANES))
                du_ref[rows, cols[hh]] = dvn[hh]
                dw_ref[rows, cols[hh]] = -dwv[hh]
            return tuple(dstates[hh] * gl_ref[hh, ci] + qdo[hh] - wdv[hh] for hh in heads)

        last = lax.fori_loop(0, chunks, step, tuple(carry_ref[hh] for hh in heads))
        for hh in heads:
            carry_ref[hh] = last[hh]

        if exchange is not None:
            @pl.when((pl.program_id(0) == groups - 1) & (pl.program_id(1) == SCAN_PARTS_BWD - 1))
            def _():
                exchange.finish(*exchange.split(ex_refs))

    hs, a_s, gl_s, st_s, gz_s, mix_s = _scan_specs(SCAN_HEADS_BWD, SCAN_PARTS_BWD, True)
    vec = pl.BlockSpec((1, GHD), lambda g, p: (0, 0))
    tok = jax.ShapeDtypeStruct((S, DGDN), F32)
    res = pl.pallas_call(
        body, name="gdn_scan_bwd", grid=(groups, SCAN_PARTS_BWD),
        in_specs=[mix_s, hs, gz_s, vec, hs, hs, hs, hs, a_s, gl_s, st_s, pl.BlockSpec(memory_space=pl.ANY)] + ex_in_specs,
        out_specs=[gz_s, hs, hs, hs, hs, a_s, gl_s, vec] + ex_out_specs,
        out_shape=[jax.ShapeDtypeStruct((S, DPROJ_PAD), BF16), tok, tok, tok, tok,
                   jax.ShapeDtypeStruct((NGH, S, CHUNK), F32), jax.ShapeDtypeStruct((NGH, NCH, 1, LANES), F32),
                   jax.ShapeDtypeStruct((1, GHD), F32)] + ex_out_shape,
        input_output_aliases={11: 0},
        scratch_shapes=[pltpu.VMEM((S // SCAN_PARTS_BWD, SCAN_HEADS_BWD * GHD), F32),
                        pltpu.VMEM((SCAN_HEADS_BWD, GHD, GHD), F32)] + ex_scratch,
        compiler_params=_cparams(),
    )(dmix, o, proj, w_norm, u, w, qd, kd, a, gl, states, dproj, *ex_in)
    return res[:8], res[8:]


def _place():
    return lax.axis_index("x"), lax.axis_index("y"), lax.axis_index("c")


def _other_chips(x, y):
    return [(1 - x, y), (x, 1 - y), (1 - x, 1 - y)]


HBM = pl.BlockSpec(memory_space=pltpu.HBM)
VMEM = pl.BlockSpec(memory_space=pltpu.VMEM)


def _half_rows(ref_or_rows, half):
    rows = ref_or_rows // 2
    return pl.ds(pl.multiple_of(half * rows, rows), rows)


class _Exchange:
    def __init__(self, inputs, out_shape, n_sems, start, finish=None, middle=None, rest=None):
        self.inputs, self.out_shape, self.n_sems, self.start = inputs, out_shape, n_sems, start
        if finish is None:
            def finish(*refs):
                middle(*refs)
                rest(*refs)
        self.finish = finish
        self.middle = middle if middle is not None else (lambda *refs: None)
        self.rest = rest if rest is not None else finish

    def sem_shapes(self):
        return [pltpu.SemaphoreType.DMA((self.n_sems,)), pltpu.SemaphoreType.DMA((self.n_sems,))]

    def split(self, refs):
        n_in, n_out = len(self.inputs), len(self.out_shape)
        return refs[:n_in], refs[n_in:n_in + n_out], refs[n_in + n_out], refs[n_in + n_out + 1]


def _allgather_exchange(shards, whole=()):
    n, nw = len(shards), len(whole)
    slots = 8

    def plan(src, outs, send_sems, recv_sems):
        x, y, c = _place()
        via_x, via_y, diagonal = _other_chips(x, y)
        id_x, id_y, id_diagonal = [2 * chip[0] + chip[1] for chip in (via_x, via_y, diagonal)]
        me, sibling = (x, y, c), (x, y, 1 - c)

        def rows_of(a, half, quarter):
            total = src[a].shape[0]
            if quarter is None:
                return _half_rows(total, half)
            return pl.ds(pl.multiple_of(half * (total // 2) + quarter * (total // 4), total // 4), total // 4)

        def copy(a, k, chip_index, half, quarter, to, from_src=False):
            rows = rows_of(a, half, quarter)
            dst = outs[a].at[chip_index, rows]
            return pltpu.make_async_remote_copy(
                src_ref=src[a].at[rows] if from_src else dst, dst_ref=dst, send_sem=send_sems.at[slots * a + k],
                recv_sem=recv_sems.at[slots * a + k], device_id=to, device_id_type=MESH)

        def whole_copy(b, k, chip_index, to):
            return pltpu.make_async_remote_copy(
                src_ref=src[n + b], dst_ref=outs[n + b].at[chip_index], send_sem=send_sems.at[slots * n + 3 * b + k],
                recv_sem=recv_sems.at[slots * n + 3 * b + k], device_id=to, device_id_type=MESH)

        first, stages, last = [], [], []
        for a in range(n):
            first += [copy(a, 0, 2 * x + y, c, None, (*via_x, c), True), copy(a, 1, 2 * x + y, c, None, (*via_y, c), True)]
            stages.append([
                (copy(a, 0, id_x, c, None, me),
                 [copy(a, 2, id_x, c, 0, (*via_y, c)), copy(a, 4, id_x, c, None, sibling)]),
                (copy(a, 1, id_y, c, None, me),
                 [copy(a, 3, id_y, c, 1, (*via_x, c)), copy(a, 5, id_y, c, None, sibling)]),
                (copy(a, 2, id_diagonal, c, 0, me), [copy(a, 6, id_diagonal, c, 0, sibling)]),
                (copy(a, 3, id_diagonal, c, 1, me), [copy(a, 7, id_diagonal, c, 1, sibling)]),
            ])
            last += [copy(a, 4, id_x, 1 - c, None, me), copy(a, 5, id_y, 1 - c, None, me),
                     copy(a, 6, id_diagonal, 1 - c, 0, me), copy(a, 7, id_diagonal, 1 - c, 1, me)]
        for b in range(nw):
            for k, (chip, index) in enumerate(((via_x, id_x), (via_y, id_y), (diagonal, id_diagonal))):
                first.append(whole_copy(b, k, 2 * x + y, (*chip, c)))
                last.append(whole_copy(b, k, index, me))
        return first, stages, last

    def start(*refs):
        for cp in plan(*refs)[0]:
            cp.start()

    def pass_on(stages, which):
        for stage in which:
            for per_shard in stages:
                lands, onward = per_shard[stage]
                lands.wait_recv()
                for cp in onward:
                    cp.start()

    def middle(*refs):
        pass_on(plan(*refs)[1], (0, 1))

    def rest(*refs):
        first, stages, last = plan(*refs)
        pass_on(stages, (2, 3))
        for cp in last:
            cp.wait_recv()
        for cp in first + [cp for per_shard in stages for _, onward in per_shard for cp in onward]:
            cp.wait_send()

    out_shape = [jax.ShapeDtypeStruct((NCHIP,) + s.shape, s.dtype) for s in list(shards) + list(whole)]
    return _Exchange(list(shards) + list(whole), out_shape, slots * n + 3 * nw, start, middle=middle, rest=rest)


def _with_own(gathered, own):
    x, y, _ = _place()
    return lax.dynamic_update_index_in_dim(gathered, own, 2 * x + y, axis=0)


def _simple_exchange(inputs, out_shape, copies_of):
    def start(*refs):
        for cp in copies_of(*refs):
            cp.start()

    def finish(*refs):
        for cp in copies_of(*refs):
            cp.wait()

    return _Exchange(list(inputs), out_shape, len(out_shape) * 3, start, finish)


def _grad_pair_sum(a, b, *, by_columns, name):
    m, n = a.shape[1], b.shape[1]
    rows, cols = (m, n // NCHIP) if by_columns else (m // NCHIP, n)
    half = rows // 2

    def body(a_ref, b_ref, out_ref, mine_ref, send_ref, recv_ref, send_sems, recv_sems):
        i = pl.program_id(0)
        x, y, c = _place()

        def share(t):
            return pltpu.make_async_remote_copy(
                src_ref=send_ref.at[t], dst_ref=recv_ref.at[t], send_sem=send_sems.at[t], recv_sem=recv_sems.at[t],
                device_id=(x, y, 1 - c), device_id_type=MESH)

        def half_block(core):
            return _dot(a_ref[:, pl.ds(pl.multiple_of(core * half, half), half)], b_ref[...], 0, 0).astype(BF16)

        def block_and_send():
            mine_ref[i % 2] = half_block(c)
            send_ref[i] = half_block(1 - c)
            share(i).start()

        def add_received():
            share(i - 1).wait_recv()
            out_ref[...] = (mine_ref[(i - 1) % 2].astype(F32) + recv_ref[i - 1].astype(F32)).astype(BF16)

        @pl.when(i == 0)
        def _():
            block_and_send()

        @pl.when((i > 0) & (i < NCHIP))
        def _():
            add_received()
            block_and_send()

        @pl.when(i == NCHIP)
        def _():
            add_received()
            for t in range(NCHIP):
                share(t).wait_send()

    def block_of(i):
        return jnp.minimum(i, NCHIP - 1)

    if by_columns:
        in_specs = [pl.BlockSpec((S, m), lambda i: (0, 0)), pl.BlockSpec((S, cols), lambda i: (0, block_of(i)))]
    else:
        in_specs = [pl.BlockSpec((S, rows), lambda i: (0, block_of(i))), pl.BlockSpec((S, n), lambda i: (0, 0))]
    return pl.pallas_call(
        body, name=name, grid=(NCHIP + 1,), in_specs=in_specs,
        out_specs=pl.BlockSpec((None, half, cols), lambda i: (jnp.maximum(i - 1, 0), 0, 0)),
        out_shape=jax.ShapeDtypeStruct((NCHIP, half, cols), BF16),
        scratch_shapes=[pltpu.VMEM((2, half, cols), BF16), pltpu.VMEM((NCHIP, half, cols), BF16),
                        pltpu.VMEM((NCHIP, half, cols), BF16), pltpu.SemaphoreType.DMA((NCHIP,)),
                        pltpu.SemaphoreType.DMA((NCHIP,))],
        compiler_params=_cparams(),
    )(a, b)


def _chip_exchange(parts):
    def copies_of(src, outs, send_sems, recv_sems):
        x, y, c = _place()
        return [pltpu.make_async_remote_copy(
            src_ref=src[a].at[2 * chip[0] + chip[1]], dst_ref=outs[a].at[k], send_sem=send_sems.at[3 * a + k],
            recv_sem=recv_sems.at[3 * a + k], device_id=(*chip, c), device_id_type=MESH)
            for a in range(len(src)) for k, chip in enumerate(_other_chips(x, y))]

    return _simple_exchange(parts, [jax.ShapeDtypeStruct((NCHIP - 1,) + p.shape[1:], p.dtype) for p in parts], copies_of)


def _chip_sum(parts, received, exchange=None):
    n = len(parts)
    steps = 4
    ex_in, ex_in_specs, ex_out_specs, ex_out_shape, ex_scratch = _hosted(exchange)
    n_ex = len(ex_in)

    def body(*refs):
        mine, theirs = refs[2 * n + n_ex:3 * n + n_ex], refs[3 * n + n_ex:4 * n + n_ex]
        ex_refs = refs[2 * n:2 * n + n_ex] + refs[4 * n + n_ex:len(refs) - n - 2]
        tiles, send_sems, recv_sems = refs[len(refs) - n - 2:len(refs) - 2], refs[-2], refs[-1]
        step = pl.program_id(0)
        if exchange is not None:
            @pl.when(step == 0)
            def _():
                exchange.start(*exchange.split(ex_refs))

        x, y, c = _place()

        def share(a, i):
            rows = tiles[a].shape[1]
            return pltpu.make_async_remote_copy(
                src_ref=tiles[a].at[i], dst_ref=theirs[a].at[pl.ds(pl.multiple_of(i * rows, rows), rows)],
                send_sem=send_sems.at[a * steps + i], recv_sem=recv_sems.at[a * steps + i], device_id=(x, y, 1 - c),
                device_id_type=MESH)

        chip = 2 * x + y
        for a in range(n):
            p, r = refs[a], refs[n + a]
            own = jnp.where(chip == 0, p[0], jnp.where(chip == 1, p[1], jnp.where(chip == 2, p[2], p[3])))
            total = ((own.astype(F32) + r[0].astype(F32)) + r[1].astype(F32)) + r[2].astype(F32)
            mine[a][...] = total
            tiles[a][step] = total
            share(a, step).start()

        @pl.when(step == steps - 1)
        def _():
            if exchange is not None:
                exchange.finish(*exchange.split(ex_refs))
            for a in range(n):
                for i in range(steps):
                    share(a, i).wait()

    def specs(arrs):
        return [pl.BlockSpec((g.shape[0], g.shape[1] // steps, g.shape[2]), lambda i: (0, i, 0)) for g in arrs]

    out_specs = [pl.BlockSpec((g.shape[1] // steps, g.shape[2]), lambda i: (i, 0)) for g in parts]
    halves = [jax.ShapeDtypeStruct(g.shape[1:], F32) for g in parts]
    res = pl.pallas_call(
        body, name="grads_chip_sum", grid=(steps,), in_specs=specs(parts) + specs(received) + ex_in_specs,
        out_specs=out_specs + [HBM] * n + ex_out_specs, out_shape=halves * 2 + ex_out_shape,
        scratch_shapes=ex_scratch + [pltpu.VMEM((steps, g.shape[1] // steps, g.shape[2]), F32) for g in parts]
        + [pltpu.SemaphoreType.DMA((n * steps,))] * 2, compiler_params=_cparams(),
    )(*parts, *received, *ex_in)
    return res[:n], res[n:2 * n], res[2 * n:]


def _adamw_math(w, g, m, v):
    nm = ADAM_B1 * m + (1.0 - ADAM_B1) * g
    nv = ADAM_B2 * v + (1.0 - ADAM_B2) * jnp.square(g)
    m_hat = nm / (1.0 - ADAM_B1 ** ADAM_STEP)
    v_hat = nv / (1.0 - ADAM_B2 ** ADAM_STEP)
    return -ADAM_LR * (m_hat / (jnp.sqrt(v_hat) + ADAM_EPS) + ADAM_WD * w), nm, nv


def _adamw_big(ws, g_mine, g_theirs, ms, vs, exchange=None):
    n = len(ws)
    steps = 8
    ex_in, ex_in_specs, ex_out_specs, ex_out_shape, ex_scratch = _hosted(exchange)

    def body(*refs):
        ex_refs = refs[5 * n:5 * n + len(ex_in)] + refs[9 * n + len(ex_in):]
        outs = refs[5 * n + len(ex_in):9 * n + len(ex_in)]
        if exchange is not None:
            @pl.when(pl.program_id(0) == 0)
            def _():
                exchange.start(*exchange.split(ex_refs))

        own_half = (pl.program_id(0) // (steps // 2)) == lax.axis_index("c")
        for a in range(n):
            g = jnp.where(own_half, refs[n + a][...], refs[2 * n + a][...])
            d, nm, nv = _adamw_math(refs[a][...], g, refs[3 * n + a][...], refs[4 * n + a][...])
            outs[a][...] = g
            outs[n + a][...] = d
            outs[2 * n + a][...] = nm
            outs[3 * n + a][...] = nv

        if exchange is not None:
            @pl.when(pl.program_id(0) == steps - 1)
            def _():
                exchange.finish(*exchange.split(ex_refs))

    specs = [pl.BlockSpec((w.shape[0] // steps, w.shape[1]), lambda i: (i, 0)) for w in ws]
    half_specs = [pl.BlockSpec((g.shape[0] // (steps // 2), g.shape[1]), lambda i: (i % (steps // 2), 0)) for g in g_mine]
    shapes = [jax.ShapeDtypeStruct(w.shape, F32) for w in ws]
    res = pl.pallas_call(
        body, name="adamw_big", grid=(steps,), in_specs=specs + half_specs * 2 + specs * 2 + ex_in_specs,
        out_specs=specs * 4 + ex_out_specs, out_shape=shapes * 4 + ex_out_shape, scratch_shapes=ex_scratch,
        compiler_params=_cparams(),
    )(*ws, *g_mine, *g_theirs, *ms, *vs, *ex_in)
    return res[:n], res[n:2 * n], res[2 * n:3 * n], res[3 * n:4 * n], res[4 * n:]


def _adamw_in(w, g_mine, g_theirs, m, v):
    half = D // 2

    def body(w_ref, gm_ref, gt_ref, m_ref, v_ref, g_out, d_out, nm_out, nv_out, g_ref):
        south = lax.axis_index("c") == 0
        g_ref[0:half, :] = jnp.where(south, gm_ref[...], gt_ref[...])
        g_ref[half:D, :] = jnp.where(south, gt_ref[...], gm_ref[...])
        g = g_ref[0:CW, :]
        d, nm, nv = _adamw_math(w_ref[...], g, m_ref[...], v_ref[...])
        g_out[...] = g
        d_out[...] = d
        nm_out[...] = nm
        nv_out[...] = nv

    spec = pl.BlockSpec((CW, LANES), lambda i: (0, i))
    half_spec = pl.BlockSpec((half, LANES), lambda i: (0, i))
    return pl.pallas_call(
        body, name="adamw_in", grid=(D // LANES,), in_specs=[spec, half_spec, half_spec, spec, spec], out_specs=[spec] * 4,
        out_shape=[jax.ShapeDtypeStruct((CW, D), F32)] * 4, scratch_shapes=[pltpu.VMEM((D, LANES), F32)],
        compiler_params=_cparams(),
    )(w, g_mine, g_theirs, m, v)


NORM_NAMES = ("pre_mix_norm", "post_mix_norm", "pre_mlp_norm", "post_mlp_norm")
SMALL_NAMES = NORM_NAMES + ("gdn_conv_w", "fox_f_bias", "gdn_dt_bias", "gdn_a_log", "fox_out_norm", "gdn_out_norm")
CONV_COLS = 3 * DGDN // NCHIP


def _small_gather(d_norms, d_conv, sums, d_fox_norm, d_gdn_norm, loss_row):
    n_arrays = 6
    n_remote = n_arrays * (NDEV - 1)

    def copies_of(src, outs, send_sems, recv_sems):
        x, y, c = _place()
        me = 4 * x + 2 * y + c

        def from_me(chip_index):
            cols = pl.ds(pl.multiple_of(chip_index * CONV_COLS, LANES), CONV_COLS)
            return [src[0], src[1].at[:, cols], src[2], src[3], src[4], src[5]]

        local = [pltpu.make_async_copy(s, outs[a].at[me], send_sems.at[n_remote + a]) for a, s in enumerate(from_me(2 * x + y))]
        remote = []
        for k in range(1, NDEV):
            px, py, pc = x ^ ((k >> 2) & 1), y ^ ((k >> 1) & 1), c ^ (k & 1)
            remote += [pltpu.make_async_remote_copy(
                src_ref=s, dst_ref=outs[a].at[me], send_sem=send_sems.at[n_arrays * (k - 1) + a],
                recv_sem=recv_sems.at[n_arrays * (k - 1) + a], device_id=(px, py, pc), device_id_type=MESH)
                for a, s in enumerate(from_me(2 * px + py))]
        return local + remote

    def start(*refs):
        for cp in copies_of(*refs):
            cp.start()

    def finish(*refs):
        for cp in copies_of(*refs):
            cp.wait()

    shapes = [(4, D), (CONV_K, CONV_COLS), (8, LANES), (1, LANES), (1, LANES), (1, LANES)]
    return _Exchange([d_norms, d_conv, sums, d_fox_norm, d_gdn_norm, loss_row],
                     [jax.ShapeDtypeStruct((NDEV,) + s, F32) for s in shapes], n_remote + n_arrays, start, finish)


def _small_adamw(gathered, ws, ms, vs):
    n = len(SMALL_NAMES)
    ng = len(gathered)

    def body(*refs):
        def total(buf):
            acc = buf[0]
            for i in range(1, NDEV):
                acc = acc + buf[i]
            return acc

        t_norms, t_conv, t_sums, t_fn, t_gn, t_loss = [total(r) for r in refs[:ng]]
        w_refs, m_refs, v_refs = refs[ng:ng + n], refs[ng + n:ng + 2 * n], refs[ng + 2 * n:ng + 3 * n]
        outs = refs[ng + 3 * n:]
        outs[4 * n][...] = t_loss
        grads = [t_norms[i:i + 1, :] for i in range(4)] + [
            t_conv, t_sums[0:1, 0:NFH], t_sums[1:2, 0:NGH], t_sums[2:3, 0:NGH], t_fn[:, 0:FHD], t_gn]
        for a in range(n):
            d, nm, nv = _adamw_math(w_refs[a][...], grads[a], m_refs[a][...], v_refs[a][...])
            outs[a][...] = grads[a]
            outs[n + a][...] = d
            outs[2 * n + a][...] = nm
            outs[3 * n + a][...] = nv

    def whole(arr):
        return pl.BlockSpec(arr.shape, lambda i: (0,) * arr.ndim)

    res = pl.pallas_call(
        body, name="small_adamw", grid=(1,), in_specs=[whole(t) for t in gathered] + [whole(w) for w in ws] * 3,
        out_specs=[whole(w) for w in ws] * 4 + [pl.BlockSpec((1, LANES), lambda i: (0, 0))],
        out_shape=[jax.ShapeDtypeStruct(w.shape, F32) for w in ws] * 4 + [jax.ShapeDtypeStruct((1, LANES), F32)],
        compiler_params=_cparams(),
    )(*gathered, *ws, *ms, *vs)
    return res[:n], res[n:2 * n], res[2 * n:3 * n], res[3 * n:4 * n], res[4 * n]


CW = DPROJ // NCHIP
PROJ_RUNS = tuple((part * DFOX + hp * LANES, part * DFOX + (hp + 1) * LANES, (3 * hp + part) * LANES)
                  for hp in range(NPAIR) for part in range(3)) + (
    (1536, 1544, BLK_SMALL * LANES), (1544, 3080, BLK_GDN * LANES), (3080, 3088, BLK_SMALL * LANES + 8),
    (3088, 3600, BLK_GZ * LANES))


def _proj_pieces():
    pieces = []
    for lo, hi, at in PROJ_RUNS:
        while lo < hi:
            j = lo // CW
            end = min(hi, (j + 1) * CW)
            pieces.append((j, lo - j * CW, at, end - lo))
            at, lo = at + end - lo, end
    return pieces


RT = 256


def _to_padded_rows(gathered):
    def body(src_ref, out_ref, blocks_ref, rows_ref):
        blocks_ref[...] = src_ref[...].astype(F32)
        rows_ref[...] = jnp.zeros_like(rows_ref)
        for j, start, at, n in _proj_pieces():
            rows_ref[at:at + n, :] = blocks_ref[j, start:start + n, :]
        out_ref[...] = rows_ref[...].astype(out_ref.dtype)

    return pl.pallas_call(
        body, name="proj_rows_in", grid=(D // RT,), in_specs=[pl.BlockSpec((NCHIP, D, RT), lambda i: (0, 0, i))],
        out_specs=pl.BlockSpec((DPROJ_PAD, RT), lambda i: (0, i)), out_shape=jax.ShapeDtypeStruct((DPROJ_PAD, D), gathered.dtype),
        scratch_shapes=[pltpu.VMEM((NCHIP, D, RT), F32), pltpu.VMEM((DPROJ_PAD, RT), F32)], compiler_params=_cparams(),
    )(gathered)


def _from_padded_rows_pair_sum(w):
    steps = D // RT
    half = D // 2

    def body(src_ref, out_ref, rows_ref, blocks_ref, mine_ref, send_ref, recv_ref, send_sems, recv_sems):
        i = pl.program_id(0)
        x, y, c = _place()

        def share(t):
            return pltpu.make_async_remote_copy(
                src_ref=send_ref.at[t], dst_ref=recv_ref.at[t], send_sem=send_sems.at[t], recv_sem=recv_sems.at[t],
                device_id=(x, y, 1 - c), device_id_type=MESH)

        def rows_of(core):
            return blocks_ref[:, pl.ds(pl.multiple_of(core * half, half), half), :]

        @pl.when(i < steps)
        def _():
            rows_ref[...] = src_ref[...].astype(F32)
            blocks_ref[...] = jnp.zeros_like(blocks_ref)
            for j, start, at, n in _proj_pieces():
                blocks_ref[j, start:start + n, :] = rows_ref[at:at + n, :]
            mine_ref[i % 2] = rows_of(c)
            send_ref[i] = rows_of(1 - c).astype(BF16)
            share(i).start()

        @pl.when(i > 0)
        def _():
            share(i - 1).wait_recv()
            out_ref[...] = (mine_ref[(i - 1) % 2] + recv_ref[i - 1].astype(F32)).astype(BF16)

        @pl.when(i == steps)
        def _():
            for t in range(steps):
                share(t).wait_send()

    tile = (NCHIP, half, RT)
    return pl.pallas_call(
        body, name="proj_rows_out_pair_sum", grid=(steps + 1,),
        in_specs=[pl.BlockSpec((DPROJ_PAD, RT), lambda i: (0, jnp.minimum(i, steps - 1)))],
        out_specs=pl.BlockSpec(tile, lambda i: (0, 0, jnp.maximum(i - 1, 0))),
        out_shape=jax.ShapeDtypeStruct((NCHIP, half, D), BF16),
        scratch_shapes=[pltpu.VMEM((DPROJ_PAD, RT), F32), pltpu.VMEM((NCHIP, D, RT), F32), pltpu.VMEM((2,) + tile, F32),
                        pltpu.VMEM((steps,) + tile, BF16), pltpu.VMEM((steps,) + tile, BF16),
                        pltpu.SemaphoreType.DMA((steps,)), pltpu.SemaphoreType.DMA((steps,))],
        compiler_params=_cparams(),
    )(w)


def _local_step(x, target, first_weights, late_weights, grad_matmul, reduce_late, reduce_in, pre_mix_norm, fox_f_bias,
                fox_out_norm, gdn_a_log, gdn_dt_bias, gdn_out_norm, post_mix_norm, pre_mlp_norm, post_mlp_norm):
    bias_vec = jnp.zeros((1, LANES), F32).at[0, 0:NFH].set(fox_f_bias).at[0, LANE_G:LANE_G + NGH].set(gdn_dt_bias)
    alog_vec = jnp.zeros((1, LANES), F32).at[0, LANE_G:LANE_G + NGH].set(gdn_a_log)
    w2 = jnp.concatenate([fox_out_norm, fox_out_norm], axis=1)

    h, first = _pre_norm(x, pre_mix_norm, exchange=first_weights[0])
    win_p, conv_w = first_weights[1](first)
    proj = _matmul(h, win_p, tb=True, tm=2048, tn=768, tk=1024, name="mm_proj", exchange=late_weights[0])
    proj, late_a = proj if late_weights[0] is not None else (proj, [])
    gates = _gates(proj, bias_vec, alog_vec)
    mix, fox_o, lse, late_b = _fox_fwd(proj, gates, w2, exchange=late_weights[1])
    qkv = _gdn_pre(proj, conv_w)
    (u, w, qd, kd, a_intra, gl, t_inv), _ = _gdn_prep(qkv, gates)
    wout, wup3 = late_weights[3](late_a, late_b)
    mix, gdn_raw, states = _gdn_scan(u, w, qd, kd, a_intra, gl, proj, gdn_out_norm, mix)
    def post_mix(acc, xv, w_post, w_pre_mlp):
        x1v = xv + acc * _rms_scale(acc) * w_post
        return acc, x1v, x1v * _rms_scale(x1v) * w_pre_mlp

    mixed, x1, h2 = _matmul(mix, wout, tm=512, tn=D, tk=1024, out_dtypes=(F32, F32, BF16), name="mm_out",
                            extra=(x, post_mix_norm, pre_mlp_norm), epilogue=post_mix)

    def relu2(acc):
        r = jnp.maximum(acc, 0.0)
        return r, r * r

    up_act = _matmul(h2, wup3, b3=True, tm=1024, tn=1024, tk=1024, out_dtypes=(BF16, BF16), epilogue=relu2,
                     name="mm_up", exchange=late_weights[2])
    (up_relu, act), late_c = up_act if late_weights[2] is not None else (up_act, [])
    wdown = late_weights[4](late_c)
    def loss_head(acc, x1v, tv, w):
        err = x1v + acc * _rms_scale(acc) * w - tv
        dx2v = err * (1.0 / D)
        dyv, dwt = _rms_bwd(acc, w, dx2v)
        part = 0.5 * jnp.sum(jnp.mean(err * err, axis=-1, keepdims=True), axis=0, keepdims=True)
        return dx2v, dyv, jnp.sum(dwt, axis=0, keepdims=True), jnp.broadcast_to(part, (1, D))

    dx2, dy, d_post_mlp, loss_wide = _matmul(
        act, wdown, tm=512, tn=D, tk=DFF, out_dtypes=(F32, BF16, F32, F32), extra=(x1, target, post_mlp_norm),
        epilogue=loss_head, n_sums=2, name="mm_down")
    loss_row = loss_wide[:, :LANES]

    dwdown = grad_matmul(act, dy, by_columns=False, name="mm_dwdown")

    def relu2_bwd(acc, r):
        return (acc * 2.0 * r.astype(F32),)

    dup = _matmul(dy, wdown, tb=True, tm=1024, tn=1024, tk=1024, out_dtypes=(BF16,), extra=(up_relu,), epilogue=relu2_bwd,
                  name="mm_dact")
    dwup3 = grad_matmul(h2, dup, by_columns=True, name="mm_dwup")

    def mid_bwd(acc, x1v, dx2v, mixedv, w_pre_mlp, w_post):
        dxa, dwm = _rms_bwd(x1v, w_pre_mlp, acc)
        dx1v = dx2v + dxa
        dm, dwp = _rms_bwd(mixedv, w_post, dx1v)
        return dx1v, dm, jnp.sum(dwm, axis=0, keepdims=True), jnp.sum(dwp, axis=0, keepdims=True)

    dx1, dmixed, d_pre_mlp, d_post_mix = _matmul(
        dup, wup3, tb=True, b3=True, tm=512, tn=D, tk=DFF, out_dtypes=(F32, BF16, F32, F32),
        extra=(x1, dx2, mixed, pre_mlp_norm, post_mix_norm), epilogue=mid_bwd, n_sums=2, name="mm_dh2")
    dwout = grad_matmul(mix, dmixed, by_columns=False, name="mm_dwout")
    dmix = _matmul(dmixed, wout, tb=True, tm=2048, tk=1024, name="mm_dmix")

    dfox, delta, d_fox_norm = _fox_norm_bwd(fox_o, dmix, w2)
    dproj, dcum_fox, reduced_a = _fox_bwd(proj, dfox, gates, lse, delta, exchange=reduce_late[0](dwout, dwup3, dwdown))
    (dproj, du, dw, dqd, dkd, da, dgl, d_gdn_norm), reduced_b = _gdn_scan_bwd(
        dmix, gdn_raw, proj, gdn_out_norm, u, w, qd, kd, a_intra, gl, states, dproj, exchange=reduce_late[1]())
    dqkv, dgates_gdn, reduced_c = _gdn_prep_bwd(qkv, gates, t_inv, du, dw, dqd, dkd, da, dgl, exchange=reduce_late[2]())
    reduced_late = (reduced_a, reduced_b, reduced_c)
    dproj, d_conv = _gdn_pre_bwd(proj, conv_w, dqkv, dproj)
    dproj, sums = _gates_bwd(proj, bias_vec, alog_vec, dgates_gdn, dcum_fox, dproj)

    dwin_p = _matmul(dproj, h, ta=True, tm=1280, tn=1024, tk=2048, out_dtypes=(BF16,), name="mm_dwin")
    exchange_in = reduce_in(dwin_p)
    dh = _matmul(dproj, win_p, tm=1024, tk=DPROJ_PAD, name="mm_dh", exchange=exchange_in)
    dh, reduced_in = dh if exchange_in is not None else (dh, [])
    grad_x, d_pre_mix = _pre_norm_bwd(dh, x, pre_mix_norm, dx1)

    d_norms = jnp.concatenate([d_pre_mix, d_post_mix, d_pre_mlp, d_post_mlp], axis=0)
    return grad_x, (d_norms, d_conv, sums, d_fox_norm, d_gdn_norm, loss_row), reduced_late, reduced_in


def kernel(x, pre_mix_norm, w_in, fox_f_bias, fox_out_norm, gdn_conv_w, gdn_a_log, gdn_dt_bias, gdn_out_norm, w_out, post_mix_norm, pre_mlp_norm, w_up, w_down, post_mlp_norm, loss_target, m_pre_mix_norm, m_w_in, m_fox_f_bias, m_fox_out_norm, m_gdn_conv_w, m_gdn_a_log, m_gdn_dt_bias, m_gdn_out_norm, m_w_out, m_post_mix_norm, m_pre_mlp_norm, m_w_up, m_w_down, m_post_mlp_norm, v_pre_mix_norm, v_w_in, v_fox_f_bias, v_fox_out_norm, v_gdn_conv_w, v_gdn_a_log, v_gdn_dt_bias, v_gdn_out_norm, v_w_out, v_post_mix_norm, v_pre_mlp_norm, v_w_up, v_w_down, v_post_mlp_norm):
    weights = dict(pre_mix_norm=pre_mix_norm, w_in=w_in, fox_f_bias=fox_f_bias, fox_out_norm=fox_out_norm, gdn_conv_w=gdn_conv_w,
                   gdn_a_log=gdn_a_log, gdn_dt_bias=gdn_dt_bias, gdn_out_norm=gdn_out_norm, w_out=w_out, post_mix_norm=post_mix_norm,
                   pre_mlp_norm=pre_mlp_norm, w_up=w_up, w_down=w_down, post_mlp_norm=post_mlp_norm)
    m_in = dict(pre_mix_norm=m_pre_mix_norm, w_in=m_w_in, fox_f_bias=m_fox_f_bias, fox_out_norm=m_fox_out_norm, gdn_conv_w=m_gdn_conv_w,
                gdn_a_log=m_gdn_a_log, gdn_dt_bias=m_gdn_dt_bias, gdn_out_norm=m_gdn_out_norm, w_out=m_w_out, post_mix_norm=m_post_mix_norm,
                pre_mlp_norm=m_pre_mlp_norm, w_up=m_w_up, w_down=m_w_down, post_mlp_norm=m_post_mlp_norm)
    v_in = dict(pre_mix_norm=v_pre_mix_norm, w_in=v_w_in, fox_f_bias=v_fox_f_bias, fox_out_norm=v_fox_out_norm, gdn_conv_w=v_gdn_conv_w,
                gdn_a_log=v_gdn_a_log, gdn_dt_bias=v_gdn_dt_bias, gdn_out_norm=v_gdn_out_norm, w_out=v_w_out, post_mix_norm=v_post_mix_norm,
                pre_mlp_norm=v_pre_mlp_norm, w_up=v_w_up, w_down=v_w_down, post_mlp_norm=v_post_mlp_norm)
    order_w = ("pre_mix_norm", "w_in", "fox_f_bias", "fox_out_norm", "gdn_conv_w", "gdn_a_log", "gdn_dt_bias", "gdn_out_norm", "w_out",
               "post_mix_norm", "pre_mlp_norm", "w_up", "w_down", "post_mlp_norm")
    big = ("w_in", "w_out", "w_up", "w_down")

    def row(v):
        return v if v.ndim == 2 else v.reshape(1, -1)

    win_shard = jnp.pad(w_in.T.astype(BF16), ((0, D - CW), (0, 0)))

    def resolve_first(gathered):
        win_g, conv_g = gathered
        return (_to_padded_rows(_with_own(win_g, win_shard)),
                _with_own(conv_g, gdn_conv_w).transpose(1, 0, 2).reshape(CONV_K, 3 * DGDN))

    late_shards = [weights[n].astype(BF16) for n in big[1:]]

    gathered_down = []

    def resolve_out_up(gathered_out, gathered_mlp):
        gathered_down.append(gathered_mlp[1])
        return _with_own(gathered_out[0], late_shards[0]).reshape(D, D), _with_own(gathered_mlp[0], late_shards[1])

    def resolve_down(_):
        return _with_own(gathered_down[0], late_shards[2]).reshape(DFF, D)

    pair_sums = {}

    def late_chip_exchange(dwout, dwup3, dwdown):
        pair_sums.update(w_out=dwout, w_up=dwup3, w_down=dwdown)
        return _chip_exchange([pair_sums["w_up"], pair_sums["w_down"]])

    def reduce_in(dwin_p):
        pair_sums["w_in"] = _from_padded_rows_pair_sum(dwin_p)
        return _chip_exchange([pair_sums["w_in"]])

    grad_x, small, received_late, received_in = _local_step(
        x[0], loss_target[0], (_allgather_exchange([win_shard], whole=[gdn_conv_w]), resolve_first),
        (_allgather_exchange(late_shards[:1]), _allgather_exchange(late_shards[1:]), None, resolve_out_up, resolve_down),
        _grad_pair_sum, (late_chip_exchange, lambda: None, lambda: _chip_exchange([pair_sums["w_out"]])),
        reduce_in, row(pre_mix_norm), fox_f_bias, row(fox_out_norm), gdn_a_log, gdn_dt_bias,
        row(gdn_out_norm), row(post_mix_norm), row(pre_mlp_norm), row(post_mlp_norm))
    received_mlp, _, received_out = received_late

    g_mine, g_theirs, small_gathered = _chip_sum(
        [pair_sums[n] for n in big], list(received_in[:1]) + list(received_out[:1]) + list(received_mlp[:2]),
        exchange=_small_gather(*small))

    g_big, d_big, nm_big, nv_big, _ = _adamw_big(
        [weights[n] for n in big[1:]], g_mine[1:], g_theirs[1:], [m_in[n] for n in big[1:]], [v_in[n] for n in big[1:]])
    in_t = _adamw_in(w_in.T, g_mine[0], g_theirs[0], m_w_in.T, v_w_in.T)
    g_small, d_small, nm_small, nv_small, loss_total = _small_adamw(
        small_gathered, [row(weights[n]) for n in SMALL_NAMES], [row(m_in[n]) for n in SMALL_NAMES],
        [row(v_in[n]) for n in SMALL_NAMES])

    grads, delta, new_m, new_v = {}, {}, {}, {}
    grads["w_in"], delta["w_in"], new_m["w_in"], new_v["w_in"] = [t.T for t in in_t]
    for i, n in enumerate(big[1:]):
        grads[n], delta[n], new_m[n], new_v[n] = g_big[i], d_big[i], nm_big[i], nv_big[i]
    for i, n in enumerate(SMALL_NAMES):
        shape = weights[n].shape
        grads[n], delta[n], new_m[n], new_v[n] = (g_small[i].reshape(shape), d_small[i].reshape(shape),
                                                  nm_small[i].reshape(shape), nv_small[i].reshape(shape))
    return (loss_total[0, 0], grad_x[None], *[grads[n] for n in order_w], *[delta[n] for n in order_w], *[new_m[n] for n in order_w],
            *[new_v[n] for n in order_w])
```

```python
import jax
import jax.numpy as jnp
from jax import lax
from jax.experimental import pallas as pl
from jax.experimental.pallas import tpu as pltpu

F32 = jnp.float32
BF16 = jnp.bfloat16
MESH = pl.DeviceIdType.MESH

S = 2048
D = 1024
NFH, FHD = 8, 64
NPAIR = NFH // 2
NGH, GHD = 4, 128
DFOX = NFH * FHD
DGDN = NGH * GHD
CHUNK = 64
NCH = S // CHUNK
CONV_K = 4
DFF = 4 * D
EPS = 1e-6
DPROJ = 3600
LANES = 128
DPROJ_PAD = 3840
BLK_GDN = 12
BLK_GZ = 24
BLK_SMALL = 28
NCHIP = 4
NDEV = 8
VMEM_LIMIT = 56 * 1024 * 1024

ADAM_LR = 0.001
ADAM_B1 = 0.9
ADAM_B2 = 0.999
ADAM_EPS = 1e-08
ADAM_WD = 0.01
ADAM_STEP = 10


def _cparams(**kw):
    return pltpu.CompilerParams(vmem_limit_bytes=VMEM_LIMIT, **kw)


def _dn(ca, cb):
    return (((ca,), (cb,)), ((), ()))


def _dot(a, b, ca=1, cb=0):
    return lax.dot_general(a.astype(BF16), b.astype(BF16), _dn(ca, cb), preferred_element_type=F32)


def _hdot(a, b, ca=1, cb=0):
    return lax.dot_general(a.astype(F32), b.astype(F32), _dn(ca, cb), precision=lax.Precision.HIGHEST,
                           preferred_element_type=F32)


def _dot3(a, b, ca=1, cb=0):
    a_hi, b_hi = a.astype(BF16), b.astype(BF16)
    a_lo, b_lo = (a - a_hi.astype(F32)).astype(BF16), (b - b_hi.astype(F32)).astype(BF16)
    dn = _dn(ca, cb)
    return (lax.dot_general(a_hi, b_hi, dn, preferred_element_type=F32)
            + (lax.dot_general(a_hi, b_lo, dn, preferred_element_type=F32)
               + lax.dot_general(a_lo, b_hi, dn, preferred_element_type=F32)))


@jax.custom_vjp
def _mm_nn(a, b):
    return _dot(a, b, 1, 0)


def _mm_nn_fwd(a, b):
    return _dot(a, b, 1, 0), (a, b)


def _mm_nn_bwd(res, g):
    a, b = res
    return _dot(g, b, 1, 1), _dot(a, g, 0, 0)


_mm_nn.defvjp(_mm_nn_fwd, _mm_nn_bwd)


@jax.custom_vjp
def _mm_nt(a, b):
    return _dot(a, b, 1, 1)


def _mm_nt_fwd(a, b):
    return _dot(a, b, 1, 1), (a, b)


def _mm_nt_bwd(res, g):
    a, b = res
    return _dot(g, b, 1, 0), _dot(g, a, 0, 0)


_mm_nt.defvjp(_mm_nt_fwd, _mm_nt_bwd)


@jax.custom_vjp
def _saved_inverse(m, t_inv):
    del m
    return t_inv


def _saved_inverse_fwd(m, t_inv):
    del m
    return t_inv, t_inv


def _saved_inverse_bwd(t_inv, g):
    return -_dot3(_dot3(t_inv, g, 0, 0), t_inv, 1, 1), jnp.zeros_like(t_inv)


_saved_inverse.defvjp(_saved_inverse_fwd, _saved_inverse_bwd)


def _sigmoid(z):
    return 1.0 / (1.0 + jnp.exp(-z))


def _softplus(z):
    return jnp.maximum(z, 0.0) + jnp.log(1.0 + jnp.exp(-jnp.abs(z)))


def _silu(z):
    return z * _sigmoid(z)


def _rms_scale(x):
    return lax.rsqrt(jnp.mean(x * x, axis=-1, keepdims=True) + EPS)


def _rms_bwd(x, w, g):
    r = _rms_scale(x)
    gw = g * w
    dx = r * gw - x * (r * r * r) * jnp.mean(gw * x, axis=-1, keepdims=True)
    return dx, g * x * r


def _matmul(a, b, *, name, ta=False, tb=False, tm=512, tn=512, tk=512, out_dtypes=(F32,), b3=False, o3=False,
            extra=(), epilogue=None, exchange=None, n_sums=0):
    m, k = (a.shape[1], a.shape[0]) if ta else a.shape
    if b3:
        n = b.shape[1] if tb else b.shape[0] * b.shape[2]
        kb = b.shape[0] * b.shape[2] if tb else b.shape[1]
    else:
        n, kb = (b.shape[0], b.shape[1]) if tb else (b.shape[1], b.shape[0])
    assert kb == k, (name, kb, k)
    tm, tn, tk = min(tm, m), min(tn, n), min(tk, k)
    assert m % tm == 0 and n % tn == 0 and k % tk == 0, (name, m, n, k, tm, tn, tk)
    nk = k // tk
    whole_k_blocks = b3 and tb and not ta and nk == 1 and b.shape[0] > 1
    n_extra = len(extra)
    n_out = len(out_dtypes)
    grid = (m // tm, n // tn, nk)
    ex_in, ex_in_specs, ex_out_specs, ex_out_shape, ex_scratch = _hosted(exchange)

    def body(*refs):
        a_ref, b_ref = refs[0], refs[1]
        extra_refs = refs[2:2 + n_extra]
        first_out = 2 + n_extra + len(ex_in)
        out_refs = refs[first_out:first_out + n_out]
        ex_refs = refs[2 + n_extra:first_out] + refs[first_out + n_out:first_out + n_out + len(ex_out_shape)] + refs[-2:]
        step = [pl.program_id(d) for d in range(3)]

        if exchange is not None:
            @pl.when((step[0] == 0) & (step[1] == 0) & (step[2] == 0))
            def _():
                exchange.start(*exchange.split(ex_refs))

        def finish(acc):
            outs = (acc,) if epilogue is None else epilogue(acc, *[r[...] for r in extra_refs])
            for o_ref, val in zip(out_refs[:n_out - n_sums], outs):
                o_ref[...] = val.astype(o_ref.dtype)
            for o_ref, val in zip(out_refs[n_out - n_sums:], outs[n_out - n_sums:]):
                @pl.when(step[0] == 0)
                def _(o_ref=o_ref, val=val):
                    o_ref[...] = val

                @pl.when(step[0] > 0)
                def _(o_ref=o_ref, val=val):
                    o_ref[...] += val

        if whole_k_blocks:
            width = b.shape[2]
            part = _dot(a_ref[:, 0:width], b_ref[0], 1, 1)
            for blk in range(1, b.shape[0]):
                part = part + _dot(a_ref[:, blk * width:(blk + 1) * width], b_ref[blk], 1, 1)
        else:
            part = _dot(a_ref[...], b_ref[...], 0 if ta else 1, 1 if tb else 0)
        if nk == 1:
            finish(part)
        else:
            acc_ref = refs[first_out + n_out + len(ex_out_shape)]

            @pl.when(step[2] == 0)
            def _():
                acc_ref[...] = part

            @pl.when(step[2] > 0)
            def _():
                acc_ref[...] += part

            @pl.when(step[2] == nk - 1)
            def _():
                finish(acc_ref[...])

        if exchange is not None:
            flat = (step[0] * grid[1] + step[1]) * nk + step[2]
            total = grid[0] * grid[1] * nk

            @pl.when(flat == total // 2)
            def _():
                exchange.middle(*exchange.split(ex_refs))

            @pl.when(flat == total - 1)
            def _():
                exchange.rest(*exchange.split(ex_refs))

    a_spec = pl.BlockSpec((tk, tm), lambda i, j, kk: (kk, i)) if ta else pl.BlockSpec((tm, tk), lambda i, j, kk: (i, kk))
    if whole_k_blocks:
        b_spec = pl.BlockSpec((b.shape[0], tn, b.shape[2]), lambda i, j, kk: (0, j, 0))
    elif b3 and tb:
        assert b.shape[2] == tk
        b_spec = pl.BlockSpec((None, tn, tk), lambda i, j, kk: (kk, j, 0))
    elif b3:
        assert b.shape[2] == tn
        b_spec = pl.BlockSpec((None, tk, tn), lambda i, j, kk: (j, kk, 0))
    elif tb:
        b_spec = pl.BlockSpec((tn, tk), lambda i, j, kk: (j, kk))
    else:
        b_spec = pl.BlockSpec((tk, tn), lambda i, j, kk: (kk, j))
    tile = pl.BlockSpec((tm, tn), lambda i, j, kk: (i, j))
    out_specs = [tile] * n_out
    out_shape = [jax.ShapeDtypeStruct((m, n), dt) for dt in out_dtypes]
    if o3:
        out_specs[0] = pl.BlockSpec((None, tm, tn), lambda i, j, kk: (j, i, 0))
        out_shape[0] = jax.ShapeDtypeStruct((n // tn, m, tn), out_dtypes[0])
    assert n_sums == 0 or tn == n
    for r in range(n_out - n_sums, n_out):
        out_specs[r] = pl.BlockSpec((1, tn), lambda i, j, kk: (0, 0))
        out_shape[r] = jax.ShapeDtypeStruct((1, n), out_dtypes[r])
    res = pl.pallas_call(
        body, name=name, grid=grid,
        in_specs=[a_spec, b_spec] + [tile if e.shape[0] == m else pl.BlockSpec((1, tn), lambda i, j, kk: (0, j)) for e in extra]
        + ex_in_specs, out_specs=out_specs + ex_out_specs,
        out_shape=out_shape + ex_out_shape,
        scratch_shapes=([pltpu.VMEM((tm, tn), F32)] if nk > 1 else []) + ex_scratch,
        compiler_params=_cparams(),
    )(a, b, *extra, *ex_in)
    if exchange is not None:
        return (res[0] if n_out == 1 else res[:n_out]), res[n_out:]
    return res[0] if n_out == 1 else res


TR = 256


def _row_spec(cols):
    return pl.BlockSpec((TR, cols), lambda i: (i, 0))


def _vec_spec(cols):
    return pl.BlockSpec((1, cols), lambda i: (0, 0))


def _pre_norm(x, w, exchange=None):
    ex_in, ex_in_specs, ex_out_specs, ex_out_shape, ex_scratch = _hosted(exchange)

    def body(*refs):
        x_ref, w_ref, h_ref = refs[0], refs[1], refs[2 + len(ex_in)]
        ex_refs = refs[2:2 + len(ex_in)] + refs[3 + len(ex_in):]
        if exchange is not None:
            @pl.when(pl.program_id(0) == 0)
            def _():
                exchange.start(*exchange.split(ex_refs))

        xv = x_ref[...]
        h_ref[...] = (xv * _rms_scale(xv) * w_ref[...]).astype(BF16)

        if exchange is not None:
            @pl.when(pl.program_id(0) == S // TR - 1)
            def _():
                exchange.finish(*exchange.split(ex_refs))

    res = pl.pallas_call(
        body, name="pre_norm", grid=(S // TR,), in_specs=[_row_spec(D), _vec_spec(D)] + ex_in_specs,
        out_specs=[_row_spec(D)] + ex_out_specs, out_shape=[jax.ShapeDtypeStruct((S, D), BF16)] + ex_out_shape,
        scratch_shapes=ex_scratch, compiler_params=_cparams(),
    )(x, w, *ex_in)
    return res[0], res[1:]


def _pre_norm_bwd(dh, x, w, dx1):
    def body(dh_ref, x_ref, w_ref, dx1_ref, dx_ref, dw_ref):
        i = pl.program_id(0)
        dxa, dwt = _rms_bwd(x_ref[...], w_ref[...], dh_ref[...])
        dx_ref[...] = dx1_ref[...] + dxa

        @pl.when(i == 0)
        def _():
            dw_ref[...] = jnp.zeros_like(dw_ref)

        dw_ref[...] += jnp.sum(dwt, axis=0, keepdims=True)

    return pl.pallas_call(
        body, name="pre_norm_bwd", grid=(S // TR,),
        in_specs=[_row_spec(D), _row_spec(D), _vec_spec(D), _row_spec(D)], out_specs=[_row_spec(D), _vec_spec(D)],
        out_shape=[jax.ShapeDtypeStruct((S, D), F32), jax.ShapeDtypeStruct((1, D), F32)], compiler_params=_cparams(),
    )(dh, x, w, dx1)


BQ = 512
NQ = S // BQ
LANE_BETA, LANE_G = 8, 12


def _gate_lanes(shape):
    lane = lax.broadcasted_iota(jnp.int32, shape, 1)
    return lane < LANE_BETA, (lane >= LANE_BETA) & (lane < LANE_G), (lane >= LANE_G) & (lane < LANE_G + NGH)


def _gates(proj, bias_vec, alog_vec):
    def body(s_ref, b_ref, a_ref, o_ref, carry_ref):
        i = pl.program_id(0)

        @pl.when(i == 0)
        def _():
            carry_ref[...] = jnp.zeros_like(carry_ref)

        z = s_ref[...] + b_ref[...]
        tail = jnp.log(1.0 + jnp.exp(-jnp.abs(z)))
        sp = jnp.maximum(z, 0.0) + tail
        lf = jnp.minimum(z, 0.0) - tail
        r = lax.broadcasted_iota(jnp.int32, (BQ, BQ), 0)
        c = lax.broadcasted_iota(jnp.int32, (BQ, BQ), 1)
        tri = (c <= r).astype(F32)
        cum = _hdot(tri, lf) + carry_ref[...]
        carry_ref[...] = cum[BQ - 1:BQ, :]
        is_fox, is_beta, is_g = _gate_lanes(z.shape)
        o_ref[...] = jnp.where(is_fox, cum, jnp.where(is_beta, _sigmoid(z), jnp.where(is_g, -jnp.exp(a_ref[...]) * sp, 0.0)))

    return pl.pallas_call(
        body, name="gates", grid=(NQ,),
        in_specs=[pl.BlockSpec((BQ, LANES), lambda i: (i, BLK_SMALL)), _vec_spec(LANES), _vec_spec(LANES)],
        out_specs=pl.BlockSpec((BQ, LANES), lambda i: (i, 0)), out_shape=jax.ShapeDtypeStruct((S, LANES), F32),
        scratch_shapes=[pltpu.VMEM((1, LANES), F32)], compiler_params=_cparams(),
    )(proj, bias_vec, alog_vec)


def _gates_bwd(proj, bias_vec, alog_vec, dgates_gdn, dcum_fox, dproj):
    def body(s_ref, b_ref, a_ref, dg_ref, dc_ref, dproj_in, dproj_ref, red_ref, carry_ref):
        del dproj_in
        i = pl.program_id(0)

        @pl.when(i == 0)
        def _():
            carry_ref[...] = jnp.zeros_like(carry_ref)
            red_ref[...] = jnp.zeros_like(red_ref)

        z = s_ref[...] + b_ref[...]
        dg = dg_ref[...] + dc_ref[...]
        r = lax.broadcasted_iota(jnp.int32, (BQ, BQ), 0)
        c = lax.broadcasted_iota(jnp.int32, (BQ, BQ), 1)
        upper = (c >= r).astype(F32)
        dlf = _hdot(upper, dg) + carry_ref[...]
        carry_ref[...] = dlf[0:1, :]
        sig = _sigmoid(z)
        g_scale = -jnp.exp(a_ref[...])
        is_fox, is_beta, is_g = _gate_lanes(z.shape)
        ds = jnp.where(is_fox, dlf * (1.0 - sig), jnp.where(is_beta, dg * sig * (1.0 - sig), jnp.where(is_g, dg * g_scale * sig, 0.0)))
        dproj_ref[:, 0:LANES] = ds.astype(BF16)
        dproj_ref[:, LANES:2 * LANES] = jnp.zeros((BQ, LANES), BF16)
        dalog = jnp.where(is_g, dg * g_scale * _softplus(z), 0.0)
        sums = jnp.sum(ds, axis=0, keepdims=True)
        red_ref[0:1, :] += jnp.where(is_fox[0:1], sums, 0.0)
        red_ref[1:2, :] += pltpu.roll(jnp.where(is_g[0:1], sums, 0.0), LANES - LANE_G, 1)
        red_ref[2:3, :] += pltpu.roll(jnp.sum(dalog, axis=0, keepdims=True), LANES - LANE_G, 1)

    blk = pl.BlockSpec((BQ, LANES), lambda i: (NQ - 1 - i, 0))
    return pl.pallas_call(
        body, name="gates_bwd", grid=(NQ,),
        in_specs=[pl.BlockSpec((BQ, LANES), lambda i: (NQ - 1 - i, BLK_SMALL)), _vec_spec(LANES), _vec_spec(LANES), blk, blk,
                  pl.BlockSpec(memory_space=pl.ANY)],
        out_specs=[pl.BlockSpec((BQ, 2 * LANES), lambda i: (NQ - 1 - i, BLK_SMALL // 2)), pl.BlockSpec((8, LANES), lambda i: (0, 0))],
        out_shape=[jax.ShapeDtypeStruct((S, DPROJ_PAD), BF16), jax.ShapeDtypeStruct((8, LANES), F32)],
        input_output_aliases={5: 0},
        scratch_shapes=[pltpu.VMEM((1, LANES), F32)], compiler_params=_cparams(),
    )(proj, bias_vec, alog_vec, dgates_gdn, dcum_fox, dproj)


FOX_SCALE = FHD ** -0.5
FOX_PAIRS = 2
FOX_PAIRS_BWD = 2


def _head_mask(e):
    lane = lax.broadcasted_iota(jnp.int32, (1, LANES), 1)
    return (lane >= e * FHD) & (lane < (e + 1) * FHD)


def _lane_col(vals, index):
    lane = lax.broadcasted_iota(jnp.int32, vals.shape, 1)
    return jnp.sum(jnp.where(lane == index, vals, 0.0), axis=1, keepdims=True)


def _sublane_row(vals, index):
    row = lax.broadcasted_iota(jnp.int32, vals.shape, 0)
    return jnp.sum(jnp.where(row == index, vals, 0.0), axis=0, keepdims=True)


def _pair_cols(c0, c1):
    lane = lax.broadcasted_iota(jnp.int32, (c0.shape[0], 2), 1)
    return jnp.where(lane == 0, c0, c1)


def _split3(x):
    hi = x.astype(BF16).astype(F32)
    rest = x - hi
    mid = rest.astype(BF16).astype(F32)
    return hi, mid, (rest - mid).astype(BF16).astype(F32)


def _fox_operand(vals, e, cum, is_query):
    lane = lax.broadcasted_iota(jnp.int32, (1, LANES), 1)
    base = (1 - e) * FHD
    parts = _split3(cum)
    own = jnp.where(_head_mask(e), vals * FOX_SCALE if is_query else vals, 0.0)
    cum_at, ones_at = (base, base + 3) if is_query else (base + 3, base)
    sign = 1.0 if is_query else -1.0
    out = own + jnp.where((lane >= ones_at) & (lane < ones_at + 3), 1.0, 0.0)
    for i, part in enumerate(parts):
        out = out + jnp.where(lane == cum_at + i, sign * part, 0.0)
    return out.astype(BF16)


def _causal_block():
    return lax.broadcasted_iota(jnp.int32, (BQ, BQ), 1) <= lax.broadcasted_iota(jnp.int32, (BQ, BQ), 0)


def _head_rms(o, masks):
    o2 = o * o
    r = [lax.rsqrt(jnp.sum(jnp.where(mk, o2, 0.0), axis=1, keepdims=True) * (1.0 / FHD) + EPS) for mk in masks]
    return jnp.where(masks[0], r[0], r[1])


def _hosted(exchange):
    if exchange is None:
        return [], [], [], [], []
    return (exchange.inputs, [HBM] * len(exchange.inputs), [HBM] * len(exchange.out_shape), exchange.out_shape,
            exchange.sem_shapes())


def _fox_fwd(proj, gates, w2, exchange=None):
    ex_in, ex_in_specs, ex_out_specs, ex_out_shape, ex_scratch = _hosted(exchange)

    n_in = 3 * FOX_PAIRS + 2
    heads = [(pp, e) for pp in range(FOX_PAIRS) for e in range(2)]

    def body(*refs):
        qkv_refs, g_ref, w_ref = refs[:3 * FOX_PAIRS], refs[3 * FOX_PAIRS], refs[3 * FOX_PAIRS + 1]
        mix_ref, o_ref, lse_ref = refs[n_in + len(ex_in):n_in + 3 + len(ex_in)]
        ka_ref, vb_ref = refs[n_in + 3 + len(ex_in) + len(ex_out_shape):n_in + 5 + len(ex_in) + len(ex_out_shape)]
        ex_refs = refs[n_in:n_in + len(ex_in)] + refs[n_in + 3 + len(ex_in):n_in + 3 + len(ex_in) + len(ex_out_shape)] + refs[-2:]
        grp, qi = pl.program_id(0), pl.program_id(1)

        def head_index(pp, e):
            return 2 * (FOX_PAIRS * grp + pp) + e

        if exchange is not None:
            @pl.when((grp == 0) & (qi == 0))
            def _():
                exchange.start(*exchange.split(ex_refs))

        @pl.when(qi == 0)
        def _():
            gt = g_ref[...]
            for pp in range(FOX_PAIRS):
                kv = qkv_refs[3 * pp + 1][...]
                for e in range(2):
                    ka_ref[2 * pp + e] = _fox_operand(kv, e, _lane_col(gt, head_index(pp, e)), False)
                vb_ref[pp] = qkv_refs[3 * pp + 2][...].astype(BF16)

        masks = [_head_mask(0), _head_mask(1)]
        gt = g_ref[pl.ds(pl.multiple_of(qi * BQ, BQ), BQ), :]
        qs = [_fox_operand(qkv_refs[3 * pp][...], e, _lane_col(gt, head_index(pp, e)), True) for pp, e in heads]
        n = range(len(heads))

        def block(kj, carry, diagonal):
            rows = pl.ds(pl.multiple_of(kj * BQ, BQ), BQ)
            s = [_dot(qs[i], ka_ref[i, rows, :], 1, 1) for i in n]
            if diagonal:
                s = [jnp.where(_causal_block(), s[i], -jnp.inf) for i in n]
            m_new = [jnp.maximum(carry[i][0], jnp.max(s[i], axis=-1, keepdims=True)) for i in n]
            p = [jnp.exp(s[i] - m_new[i]) for i in n]
            alpha = [jnp.exp(carry[i][0] - m_new[i]) for i in n]
            l_new = [alpha[i] * carry[i][1] + jnp.sum(p[i], axis=-1, keepdims=True) for i in n]
            pv = [_dot(p[i], vb_ref[heads[i][0], rows, :]) for i in n]
            return tuple((m_new[i], l_new[i], alpha[i] * carry[i][2] + pv[i]) for i in n)

        one = (jnp.full((BQ, 1), -jnp.inf, F32), jnp.zeros((BQ, 1), F32), jnp.zeros((BQ, LANES), F32))
        below = lax.fori_loop(0, qi, lambda kj, carry: block(kj, carry, False), (one,) * len(heads))
        done = block(qi, below, True)
        for pp in range(FOX_PAIRS):
            (m0, l0, a0), (m1, l1, a1) = done[2 * pp], done[2 * pp + 1]
            o = jnp.where(masks[0], a0 / l0, a1 / l1)
            cols = slice(pp * LANES, (pp + 1) * LANES)
            o_ref[:, cols] = o
            mix_ref[:, cols] = (o * _head_rms(o, masks) * w_ref[...]).astype(BF16)
            lse_ref[pp] = _pair_cols(m0 + jnp.log(l0), m1 + jnp.log(l1))

        if exchange is not None:
            @pl.when((grp == NPAIR // FOX_PAIRS // 2) & (qi == 0))
            def _():
                exchange.middle(*exchange.split(ex_refs))

            @pl.when((grp == NPAIR // FOX_PAIRS - 1) & (qi == NQ - 1))
            def _():
                exchange.rest(*exchange.split(ex_refs))

    qkv_specs = []
    for pp in range(FOX_PAIRS):
        qkv_specs.append(pl.BlockSpec((BQ, LANES), lambda g, i, pp=pp: (i, 3 * (FOX_PAIRS * g + pp))))
        qkv_specs.append(pl.BlockSpec((S, LANES), lambda g, i, pp=pp: (0, 3 * (FOX_PAIRS * g + pp) + 1)))
        qkv_specs.append(pl.BlockSpec((S, LANES), lambda g, i, pp=pp: (0, 3 * (FOX_PAIRS * g + pp) + 2)))
    blk = pl.BlockSpec((BQ, FOX_PAIRS * LANES), lambda g, i: (i, g))
    res = pl.pallas_call(
        body, name="fox_fwd", grid=(NPAIR // FOX_PAIRS, NQ),
        in_specs=qkv_specs + [pl.BlockSpec((S, LANES), lambda g, i: (0, 0)), pl.BlockSpec((1, LANES), lambda g, i: (0, 0))]
        + ex_in_specs,
        out_specs=[blk, blk, pl.BlockSpec((FOX_PAIRS, BQ, 2), lambda g, i: (g, i, 0))] + ex_out_specs,
        out_shape=[jax.ShapeDtypeStruct((S, D), BF16), jax.ShapeDtypeStruct((S, DFOX), F32),
                   jax.ShapeDtypeStruct((NPAIR, S, 2), F32)] + ex_out_shape,
        scratch_shapes=[pltpu.VMEM((2 * FOX_PAIRS, S, LANES), BF16), pltpu.VMEM((FOX_PAIRS, S, LANES), BF16)] + ex_scratch,
        compiler_params=_cparams(),
    )(*([proj] * (3 * FOX_PAIRS)), gates, w2, *ex_in)
    return res[0], res[1], res[2], res[3:]


def _fox_norm_bwd(o, dmix, w2):
    def body(o_ref, g_ref, w_ref, do_ref, dl_ref, dw_ref):
        hp, qi = pl.program_id(0), pl.program_id(1)
        masks = [_head_mask(0), _head_mask(1)]
        ov = o_ref[...]
        g = g_ref[...]
        r = _head_rms(ov, masks)
        gw = g * w_ref[...]
        gwo = gw * ov
        mean = [jnp.sum(jnp.where(mk, gwo, 0.0), axis=1, keepdims=True) * (1.0 / FHD) for mk in masks]
        do = r * gw - ov * (r * r * r) * jnp.where(masks[0], mean[0], mean[1])
        do_ref[...] = do.astype(BF16)
        doo = do * ov
        dl_ref[...] = _pair_cols(*[jnp.sum(jnp.where(mk, doo, 0.0), axis=1, keepdims=True) for mk in masks])

        @pl.when((hp == 0) & (qi == 0))
        def _():
            dw_ref[...] = jnp.zeros_like(dw_ref)

        dw_ref[...] += jnp.sum(g * ov * r, axis=0, keepdims=True)

        @pl.when((hp == NPAIR - 1) & (qi == NQ - 1))
        def _():
            dw = dw_ref[...]
            dw_ref[...] = dw + pltpu.roll(dw, FHD, 1)

    blk = pl.BlockSpec((BQ, LANES), lambda hp, i: (i, hp))
    vec = pl.BlockSpec((1, LANES), lambda hp, i: (0, 0))
    return pl.pallas_call(
        body, name="fox_norm_bwd", grid=(NPAIR, NQ), in_specs=[blk, blk, vec],
        out_specs=[blk, pl.BlockSpec((None, BQ, 2), lambda hp, i: (hp, i, 0)), vec],
        out_shape=[jax.ShapeDtypeStruct((S, DFOX), BF16), jax.ShapeDtypeStruct((NPAIR, S, 2), F32),
                   jax.ShapeDtypeStruct((1, LANES), F32)],
        compiler_params=_cparams(),
    )(o, dmix, w2)


def _fox_bwd(proj, do, gates, lse, delta, exchange=None):
    ex_in, ex_in_specs, ex_out_specs, ex_out_shape, ex_scratch = _hosted(exchange)

    pg = FOX_PAIRS_BWD
    n_in = 3 * pg + 4
    heads = [(pp, e) for pp in range(pg) for e in range(2)]

    def body(*refs):
        qkv_refs = refs[:3 * pg]
        do_ref, g_ref, lse_ref, dl_ref = refs[3 * pg:n_in]
        dproj_ref, dc_ref = refs[n_in + len(ex_in):n_in + 2 + len(ex_in)]
        qa_ref, dq_ref = refs[n_in + 2 + len(ex_in) + len(ex_out_shape):n_in + 4 + len(ex_in) + len(ex_out_shape)]
        ex_refs = refs[n_in:n_in + len(ex_in)] + refs[n_in + 2 + len(ex_in):n_in + 2 + len(ex_in) + len(ex_out_shape)] + refs[-2:]
        grp, kj = pl.program_id(0), pl.program_id(1)

        def head_index(pp, e):
            return 2 * (pg * grp + pp) + e

        if exchange is not None:
            @pl.when((grp == 0) & (kj == 0))
            def _():
                exchange.start(*exchange.split(ex_refs))

        @pl.when(kj == 0)
        def _():
            gt = g_ref[...]
            for pp in range(pg):
                qv = qkv_refs[3 * pp][...]
                for e in range(2):
                    qa_ref[2 * pp + e] = _fox_operand(qv, e, _lane_col(gt, head_index(pp, e)), True)
            dq_ref[...] = jnp.zeros_like(dq_ref)

        @pl.when((grp == 0) & (kj == 0))
        def _():
            dc_ref[...] = jnp.zeros_like(dc_ref)

        masks = [_head_mask(0), _head_mask(1)]
        krows = pl.ds(pl.multiple_of(kj * BQ, BQ), BQ)
        gk = g_ref[krows, :]
        kas = [_fox_operand(qkv_refs[3 * pp + 1][...], e, _lane_col(gk, head_index(pp, e)), False) for pp, e in heads]
        vbs = [qkv_refs[3 * pp + 2][...].astype(BF16) for pp in range(pg)]
        lane = lax.broadcasted_iota(jnp.int32, (BQ, LANES), 1)
        n = range(len(heads))

        def block(qi, carry, diagonal):
            dks, dvs, css = carry
            rows = pl.ds(pl.multiple_of(qi * BQ, BQ), BQ)
            qa = [qa_ref[i, rows, :] for i in n]
            s = [_dot(qa[i], kas[i], 1, 1) for i in n]
            if diagonal:
                s = [jnp.where(_causal_block(), s[i], -jnp.inf) for i in n]
            dov = [do_ref[rows, pp * LANES:(pp + 1) * LANES] for pp in range(pg)]
            doe = [jnp.where(masks[e], dov[pp], jnp.zeros_like(dov[pp])) for pp, e in heads]
            lse2 = [lse_ref[pp, rows, :] for pp in range(pg)]
            dl2 = [dl_ref[pp, rows, :] for pp in range(pg)]
            p = [jnp.exp(s[i] - _lane_col(lse2[heads[i][0]], heads[i][1])) for i in n]
            dp = [_dot(doe[i], vbs[heads[i][0]], 1, 1) for i in n]
            ds = [p[i] * (dp[i] - _lane_col(dl2[heads[i][0]], heads[i][1])) for i in n]
            dv_part = [_dot(p[i], doe[i], 0, 0) for i in n]
            dk_part = [_dot(ds[i], jnp.where(masks[heads[i][1]], qa[i], jnp.zeros_like(qa[i])), 0, 0) for i in n]
            dq_part = [jnp.where(masks[heads[i][1]], _dot(ds[i], kas[i]), 0.0) for i in n]
            css = tuple(css[i] + jnp.sum(ds[i], axis=0, keepdims=True) for i in n)
            dc = jnp.zeros((BQ, LANES), F32)
            for i in n:
                dc = dc + jnp.where(lane == head_index(*heads[i]), jnp.sum(ds[i], axis=1, keepdims=True), 0.0)
            for pp in range(pg):
                dq_ref[pp, rows, :] += (dq_part[2 * pp] + dq_part[2 * pp + 1]) * FOX_SCALE
            dc_ref[rows, :] += dc
            dks = tuple(dks[pp] + dk_part[2 * pp] + dk_part[2 * pp + 1] for pp in range(pg))
            dvs = tuple(dvs[pp] + dv_part[2 * pp] + dv_part[2 * pp + 1] for pp in range(pg))
            return dks, dvs, css

        zero = jnp.zeros((BQ, LANES), F32)
        first = block(kj, ((zero,) * pg, (zero,) * pg, (jnp.zeros((1, BQ), F32),) * len(heads)), True)
        dks, dvs, css = lax.fori_loop(kj + 1, NQ, lambda qi, carry: block(qi, carry, False), first)
        r = lax.broadcasted_iota(jnp.int32, (BQ, BQ), 0)
        c = lax.broadcasted_iota(jnp.int32, (BQ, BQ), 1)
        dcol = jnp.zeros((BQ, LANES), F32)
        for i in n:
            col = jnp.sum(jnp.where(r == c, css[i], 0.0), axis=1, keepdims=True)
            dcol = dcol + jnp.where(lane == head_index(*heads[i]), col, 0.0)
        dc_ref[krows, :] -= dcol
        for pp in range(pg):
            base = 3 * pp * LANES
            dproj_ref[krows, base + LANES:base + 2 * LANES] = dks[pp].astype(BF16)
            dproj_ref[krows, base + 2 * LANES:base + 3 * LANES] = dvs[pp].astype(BF16)

        @pl.when(kj == NQ - 1)
        def _():
            for pp in range(pg):
                dproj_ref[:, 3 * pp * LANES:(3 * pp + 1) * LANES] = dq_ref[pp].astype(BF16)

        if exchange is not None:
            @pl.when((grp == NPAIR // pg // 2) & (kj == 0))
            def _():
                exchange.middle(*exchange.split(ex_refs))

            @pl.when((grp == NPAIR // pg - 1) & (kj == NQ - 1))
            def _():
                exchange.rest(*exchange.split(ex_refs))

    qkv_specs = []
    for pp in range(pg):
        qkv_specs.append(pl.BlockSpec((S, LANES), lambda g, j, pp=pp: (0, 3 * (pg * g + pp))))
        qkv_specs.append(pl.BlockSpec((BQ, LANES), lambda g, j, pp=pp: (j, 3 * (pg * g + pp) + 1)))
        qkv_specs.append(pl.BlockSpec((BQ, LANES), lambda g, j, pp=pp: (j, 3 * (pg * g + pp) + 2)))
    pair = pl.BlockSpec((pg, S, 2), lambda g, j: (g, 0, 0))
    res = pl.pallas_call(
        body, name="fox_bwd", grid=(NPAIR // pg, NQ),
        in_specs=qkv_specs + [pl.BlockSpec((S, pg * LANES), lambda g, j: (0, g)), pl.BlockSpec((S, LANES), lambda g, j: (0, 0)),
                              pair, pair] + ex_in_specs,
        out_specs=[pl.BlockSpec((S, 3 * pg * LANES), lambda g, j: (0, g)), pl.BlockSpec((S, LANES), lambda g, j: (0, 0))]
        + ex_out_specs,
        out_shape=[jax.ShapeDtypeStruct((S, DPROJ_PAD), BF16), jax.ShapeDtypeStruct((S, LANES), F32)] + ex_out_shape,
        scratch_shapes=[pltpu.VMEM((2 * pg, S, LANES), BF16), pltpu.VMEM((pg, S, LANES), F32)] + ex_scratch,
        compiler_params=_cparams(),
    )(*([proj] * (3 * pg)), do, gates, lse, delta, *ex_in)
    return res[0], res[1], res[2:]


NQKV = 3 * NGH
GDN_QSCALE = GHD ** -0.5


def _shift_down(x, s):
    if s == 0:
        return x
    row = lax.broadcasted_iota(jnp.int32, x.shape, 0)
    return jnp.where(row >= s, pltpu.roll(x, s, 0), 0.0)


def _shift_up(x, s):
    if s == 0:
        return x
    n = x.shape[0]
    row = lax.broadcasted_iota(jnp.int32, x.shape, 0)
    return jnp.where(row < n - s, pltpu.roll(x, n - s, 0), 0.0)


def _conv_taps(xv):
    return [_shift_down(xv, CONV_K - 1 - j) for j in range(CONV_K)]


def _conv_pre(taps, wv):
    pre = taps[CONV_K - 1] * wv[CONV_K - 1:CONV_K, :]
    for j in range(CONV_K - 1):
        pre = pre + taps[j] * wv[j:j + 1, :]
    return pre


def _l2_factors(b):
    return b < 2 * NGH, jnp.where(b < NGH, GDN_QSCALE, 1.0)


def _gdn_pre(proj, conv_w):
    def body(x_ref, w_ref, o_ref):
        b = pl.program_id(0)
        c = _silu(_conv_pre(_conv_taps(x_ref[...]), w_ref[...]))
        normed, scale = _l2_factors(b)
        rs = lax.rsqrt(jnp.sum(c * c, axis=-1, keepdims=True) + EPS)
        o_ref[...] = c * jnp.where(normed, rs, 1.0) * scale

    return pl.pallas_call(
        body, name="gdn_pre", grid=(NQKV,),
        in_specs=[pl.BlockSpec((S, GHD), lambda b: (0, BLK_GDN + b)), pl.BlockSpec((CONV_K, GHD), lambda b: (0, b))],
        out_specs=pl.BlockSpec((S, GHD), lambda b: (0, b)),
        out_shape=jax.ShapeDtypeStruct((S, NQKV * GHD), F32), compiler_params=_cparams(),
    )(proj, conv_w)


def _gdn_pre_bwd(proj, conv_w, dqkv, dproj):
    def body(x_ref, w_ref, dy_ref, dproj_in, dx_ref, dw_ref):
        del dproj_in
        b = pl.program_id(0)
        taps = _conv_taps(x_ref[...])
        wv = w_ref[...]
        pre = _conv_pre(taps, wv)
        sig = _sigmoid(pre)
        c = pre * sig
        normed, scale = _l2_factors(b)
        g = dy_ref[...] * scale
        rs = lax.rsqrt(jnp.sum(c * c, axis=-1, keepdims=True) + EPS)
        dc_n = rs * g - c * (rs * rs * rs) * jnp.sum(g * c, axis=-1, keepdims=True)
        dc = jnp.where(normed, dc_n, g)
        dpre = dc * sig * (1.0 + pre * (1.0 - sig))
        dx = dpre * wv[CONV_K - 1:CONV_K, :]
        for j in range(CONV_K - 1):
            dx = dx + _shift_up(dpre, CONV_K - 1 - j) * wv[j:j + 1, :]
        dx_ref[...] = dx.astype(BF16)
        for j in range(CONV_K):
            dw_ref[j:j + 1, :] = jnp.sum(dpre * taps[j], axis=0, keepdims=True)

    return pl.pallas_call(
        body, name="gdn_pre_bwd", grid=(NQKV,),
        in_specs=[pl.BlockSpec((S, GHD), lambda b: (0, BLK_GDN + b)), pl.BlockSpec((CONV_K, GHD), lambda b: (0, b)),
                  pl.BlockSpec((None, S, GHD), lambda b: (b // NGH, 0, b % NGH)), pl.BlockSpec(memory_space=pl.ANY)],
        out_specs=[pl.BlockSpec((S, GHD), lambda b: (0, BLK_GDN + b)), pl.BlockSpec((CONV_K, GHD), lambda b: (0, b))],
        out_shape=[jax.ShapeDtypeStruct((S, DPROJ_PAD), BF16), jax.ShapeDtypeStruct((CONV_K, NQKV * GHD), F32)],
        input_output_aliases={3: 0}, compiler_params=_cparams(),
    )(proj, conv_w, dqkv, dproj)


CB = 16
NCB = NCH // CB


def _chunk_prep(qs, ks, vs, gcols, bcols, t_saved=None):
    n = range(len(qs))
    r = lax.broadcasted_iota(jnp.int32, (CHUNK, CHUNK), 0)
    c = lax.broadcasted_iota(jnp.int32, (CHUNK, CHUNK), 1)
    incl = c <= r
    eye = (r == c).astype(F32)
    grow = [jnp.sum(gcols[i] * eye, axis=0, keepdims=True) for i in n]
    gc_col = [jnp.sum(jnp.where(incl, grow[i], 0.0), axis=1, keepdims=True) for i in n]
    gc_row = [jnp.sum(jnp.where(r <= c, gcols[i], 0.0), axis=0, keepdims=True) for i in n]
    decay = [jnp.exp(jnp.where(incl, gc_col[i] - gc_row[i], -jnp.inf)) for i in n]
    kb = [ks[i] * bcols[i] for i in n]
    vb = [vs[i] * bcols[i] for i in n]
    kk = [_mm_nt(kb[i], ks[i]) for i in n]
    m = [jnp.where(c < r, kk[i] * decay[i], 0.0) for i in n]
    if t_saved is None:
        t_inv = [eye - m[i] for i in n]
        p = [_dot3(m[i], m[i]) for i in n]
        for step in range(5):
            t_inv = [t_inv[i] + _dot3(t_inv[i], p[i]) for i in n]
            if step < 4:
                p = [_dot3(p[i], p[i]) for i in n]
    else:
        t_inv = [_saved_inverse(m[i], t_saved[i]) for i in n]
    egc = [jnp.exp(gc_col[i]) for i in n]
    u = [_mm_nn(t_inv[i], vb[i]) for i in n]
    w = [_mm_nn(t_inv[i], kb[i] * egc[i]) for i in n]
    qk = [_mm_nt(qs[i], ks[i]) for i in n]
    gc_last = [gc_col[i][CHUNK - 1:CHUNK, :] for i in n]
    return [(u[i], w[i], qk[i] * decay[i], qs[i] * egc[i], ks[i] * jnp.exp(gc_last[i] - gc_col[i]), jnp.exp(gc_last[i]),
             t_inv[i]) for i in n]


def _prep_specs():
    rows = CB * CHUNK
    qs = pl.BlockSpec((rows, GHD), lambda i, h: (i, h))
    ks = pl.BlockSpec((rows, GHD), lambda i, h: (i, NGH + h))
    vs = pl.BlockSpec((rows, GHD), lambda i, h: (i, 2 * NGH + h))
    gs = pl.BlockSpec((rows, LANES), lambda i, h: (i, 0))
    a_s = pl.BlockSpec((None, rows, CHUNK), lambda i, h: (h, i, 0))
    gl_s = pl.BlockSpec((None, CB, 1, LANES), lambda i, h: (h, i, 0, 0))
    return qs, ks, vs, gs, a_s, gl_s


def _gdn_prep(qkv, gates, exchange=None):
    ex_in, ex_in_specs, ex_out_specs, ex_out_shape, ex_scratch = _hosted(exchange)

    def body(*refs):
        q_ref, k_ref, v_ref, g_ref = refs[:4]
        u_ref, w_ref, qd_ref, kd_ref, a_ref, gl_ref, t_ref = refs[4 + len(ex_in):11 + len(ex_in)]
        ex_refs = refs[4:4 + len(ex_in)] + refs[11 + len(ex_in):]
        h = pl.program_id(1)

        if exchange is not None:
            @pl.when((pl.program_id(0) == 0) & (h == 0))
            def _():
                exchange.start(*exchange.split(ex_refs))

        chunks = [pl.ds(cidx * CHUNK, CHUNK) for cidx in range(CB)]
        gts = [g_ref[rows, :] for rows in chunks]
        outs = _chunk_prep([q_ref[rows, :] for rows in chunks], [k_ref[rows, :] for rows in chunks],
                           [v_ref[rows, :] for rows in chunks], [_lane_col(gt, LANE_G + h) for gt in gts],
                           [_lane_col(gt, LANE_BETA + h) for gt in gts])
        for cidx, rows in enumerate(chunks):
            u, w, a, qd, kd, gl, t_inv = outs[cidx]
            u_ref[rows, :] = u
            w_ref[rows, :] = w
            qd_ref[rows, :] = qd
            kd_ref[rows, :] = kd
            a_ref[rows, :] = a
            t_ref[rows, :] = t_inv
            gl_ref[cidx] = jnp.broadcast_to(gl, (1, LANES))

        if exchange is not None:
            @pl.when((pl.program_id(0) == NCB // 2) & (h == 0))
            def _():
                exchange.middle(*exchange.split(ex_refs))

            @pl.when((pl.program_id(0) == NCB - 1) & (h == NGH - 1))
            def _():
                exchange.rest(*exchange.split(ex_refs))

    qs, ks, vs, gs, a_s, gl_s = _prep_specs()
    tok = jax.ShapeDtypeStruct((S, DGDN), F32)
    sq = jax.ShapeDtypeStruct((NGH, S, CHUNK), F32)
    res = pl.pallas_call(
        body, name="gdn_prep", grid=(NCB, NGH), in_specs=[qs, ks, vs, gs] + ex_in_specs,
        out_specs=[qs, qs, qs, qs, a_s, gl_s, a_s] + ex_out_specs,
        out_shape=[tok, tok, tok, tok, sq, jax.ShapeDtypeStruct((NGH, NCH, 1, LANES), F32), sq] + ex_out_shape,
        scratch_shapes=ex_scratch, compiler_params=_cparams(),
    )(qkv, qkv, qkv, gates, *ex_in)
    return res[:7], res[7:]


def _gdn_prep_bwd(qkv, gates, t_inv, du, dw, dqd, dkd, da, dgl, exchange=None):
    ex_in, ex_in_specs, ex_out_specs, ex_out_shape, ex_scratch = _hosted(exchange)

    def body(*refs):
        q_ref, k_ref, v_ref, g_ref, t_ref, du_ref, dw_ref, dqd_ref, dkd_ref, da_ref, dgl_ref = refs[:11]
        dqkv_ref, dg_ref = refs[11 + len(ex_in):13 + len(ex_in)]
        ex_refs = refs[11:11 + len(ex_in)] + refs[13 + len(ex_in):]
        h = pl.program_id(1)

        if exchange is not None:
            @pl.when((pl.program_id(0) == 0) & (h == 0))
            def _():
                exchange.start(*exchange.split(ex_refs))

        @pl.when(h == 0)
        def _():
            dg_ref[...] = jnp.zeros_like(dg_ref)

        lane = lax.broadcasted_iota(jnp.int32, (CHUNK, LANES), 1)
        chunks = [pl.ds(cidx * CHUNK, CHUNK) for cidx in range(CB)]
        gts = [g_ref[rows, :] for rows in chunks]
        t_saved = [t_ref[rows, :] for rows in chunks]
        _, vjp = jax.vjp(lambda *args: [o[:6] for o in _chunk_prep(*args, t_saved=t_saved)],
                         [q_ref[rows, :] for rows in chunks], [k_ref[rows, :] for rows in chunks],
                         [v_ref[rows, :] for rows in chunks], [_lane_col(gt, LANE_G + h) for gt in gts],
                         [_lane_col(gt, LANE_BETA + h) for gt in gts])
        dqs, dks, dvs, dgcs, dbcs = vjp([(du_ref[rows, :], dw_ref[rows, :], da_ref[rows, :], dqd_ref[rows, :],
                                          dkd_ref[rows, :], dgl_ref[cidx][:, 0:1]) for cidx, rows in enumerate(chunks)])
        for cidx, rows in enumerate(chunks):
            dq, dk, dv, dgc, dbc = dqs[cidx], dks[cidx], dvs[cidx], dgcs[cidx], dbcs[cidx]
            dqkv_ref[0, rows, :] = dq
            dqkv_ref[1, rows, :] = dk
            dqkv_ref[2, rows, :] = dv
            dg_ref[rows, :] += jnp.where(lane == LANE_G + h, dgc, 0.0) + jnp.where(lane == LANE_BETA + h, dbc, 0.0)

        if exchange is not None:
            @pl.when((pl.program_id(0) == NCB - 1) & (h == NGH - 1))
            def _():
                exchange.finish(*exchange.split(ex_refs))

    qs, ks, vs, gs, a_s, gl_s = _prep_specs()
    res = pl.pallas_call(
        body, name="gdn_prep_bwd", grid=(NCB, NGH), in_specs=[qs, ks, vs, gs, a_s, qs, qs, qs, qs, a_s, gl_s] + ex_in_specs,
        out_specs=[pl.BlockSpec((3, CB * CHUNK, GHD), lambda i, h: (0, i, h)), gs] + ex_out_specs,
        out_shape=[jax.ShapeDtypeStruct((3, S, DGDN), F32), jax.ShapeDtypeStruct((S, LANES), F32)] + ex_out_shape,
        scratch_shapes=ex_scratch, compiler_params=_cparams(),
    )(qkv, qkv, qkv, gates, t_inv, du, dw, dqd, dkd, da, dgl, *ex_in)
    return res[0], res[1], res[2:]


def _scan_specs(nh, parts, reverse):
    wide, rows, chunks = nh * GHD, S // parts, NCH // parts

    def part(p):
        return parts - 1 - p if reverse else p

    hs = pl.BlockSpec((rows, wide), lambda g, p: (part(p), g))
    a_s = pl.BlockSpec((nh, rows, CHUNK), lambda g, p: (g, part(p), 0))
    gl_s = pl.BlockSpec((nh, chunks, 1, LANES), lambda g, p: (g, part(p), 0, 0))
    st_s = pl.BlockSpec((nh, chunks, GHD, GHD), lambda g, p: (g, part(p), 0, 0))
    gz_s = pl.BlockSpec((rows, wide), lambda g, p: (part(p), BLK_GZ // nh + g))
    mix_s = pl.BlockSpec((rows, wide), lambda g, p: (part(p), NPAIR // nh + g))
    return hs, a_s, gl_s, st_s, gz_s, mix_s


def _head_cols(hh):
    return slice(hh * GHD, (hh + 1) * GHD)


SCAN_HEADS, SCAN_PARTS = 4, 2
SCAN_HEADS_BWD, SCAN_PARTS_BWD = 4, 4


def _gdn_scan(u, w, qd, kd, a, gl, proj, w_norm, mix):
    heads = range(SCAN_HEADS)

    def body(u_ref, w_ref, qd_ref, kd_ref, a_ref, gl_ref, z_ref, wn_ref, mix_in, mix_ref, o_ref, st_ref, carry_ref):
        del mix_in

        @pl.when(pl.program_id(1) == 0)
        def _():
            carry_ref[...] = jnp.zeros_like(carry_ref)

        def step(ci, states):
            rows = pl.ds(pl.multiple_of(ci * CHUNK, CHUNK), CHUNK)
            for hh in heads:
                st_ref[hh, ci] = states[hh]
            ws = [_dot(w_ref[rows, _head_cols(hh)], states[hh]) for hh in heads]
            qs = [_dot(qd_ref[rows, _head_cols(hh)], states[hh]) for hh in heads]
            vn = [u_ref[rows, _head_cols(hh)] - ws[hh] for hh in heads]
            av = [_dot(a_ref[hh, rows, :], vn[hh]) for hh in heads]
            kv = [_dot(kd_ref[rows, _head_cols(hh)], vn[hh], 0, 0) for hh in heads]
            for hh in heads:
                o_ref[rows, _head_cols(hh)] = qs[hh] + av[hh]
            return tuple(states[hh] * gl_ref[hh, ci] + kv[hh] for hh in heads)

        last = lax.fori_loop(0, NCH // SCAN_PARTS, step, tuple(carry_ref[hh] for hh in heads))
        for hh in heads:
            carry_ref[hh] = last[hh]
            ov = o_ref[:, _head_cols(hh)]
            mix_ref[:, _head_cols(hh)] = (ov * _rms_scale(ov) * wn_ref[...] * _silu(z_ref[:, _head_cols(hh)])).astype(BF16)

    hs, a_s, gl_s, st_s, gz_s, mix_s = _scan_specs(SCAN_HEADS, SCAN_PARTS, False)
    return pl.pallas_call(
        body, name="gdn_scan", grid=(NGH // SCAN_HEADS, SCAN_PARTS),
        in_specs=[hs, hs, hs, hs, a_s, gl_s, gz_s, pl.BlockSpec((1, GHD), lambda g, p: (0, 0)),
                  pl.BlockSpec(memory_space=pl.ANY)],
        out_specs=[mix_s, hs, st_s],
        out_shape=[jax.ShapeDtypeStruct((S, D), BF16), jax.ShapeDtypeStruct((S, DGDN), F32),
                   jax.ShapeDtypeStruct((NGH, NCH, GHD, GHD), F32)],
        input_output_aliases={8: 0}, scratch_shapes=[pltpu.VMEM((SCAN_HEADS, GHD, GHD), F32)], compiler_params=_cparams(),
    )(u, w, qd, kd, a, gl, proj, w_norm, mix)


def _gdn_scan_bwd(dmix, o, proj, w_norm, u, w, qd, kd, a, gl, states, dproj, exchange=None):
    ex_in, ex_in_specs, ex_out_specs, ex_out_shape, ex_scratch = _hosted(exchange)
    groups = NGH // SCAN_HEADS_BWD

    def body(*refs):
        dy_ref, o_ref, z_ref, wn_ref, u_ref, w_ref, qd_ref, kd_ref, a_ref, gl_ref, st_ref = refs[:11]
        dz_ref, du_ref, dw_ref, dqd_ref, dkd_ref, da_ref, dgl_ref, dwn_ref = refs[12 + len(ex_in):20 + len(ex_in)]
        do_ref, carry_ref = refs[20 + len(ex_in) + len(ex_out_shape):22 + len(ex_in) + len(ex_out_shape)]
        ex_refs = refs[12:12 + len(ex_in)] + refs[20 + len(ex_in):20 + len(ex_in) + len(ex_out_shape)] + refs[-2:]
        heads = range(SCAN_HEADS_BWD)
        chunks = NCH // SCAN_PARTS_BWD

        if exchange is not None:
            @pl.when((pl.program_id(0) == 0) & (pl.program_id(1) == 0))
            def _():
                exchange.start(*exchange.split(ex_refs))

        @pl.when((pl.program_id(0) == 0) & (pl.program_id(1) == 0))
        def _():
            dwn_ref[...] = jnp.zeros_like(dwn_ref)

        @pl.when(pl.program_id(1) == 0)
        def _():
            carry_ref[...] = jnp.zeros_like(carry_ref)

        wn = wn_ref[...]
        for hh in heads:
            c = _head_cols(hh)
            ov = o_ref[:, c]
            zv = z_ref[:, c]
            g = dy_ref[:, c]
            sig = _sigmoid(zv)
            dz_ref[:, c] = (g * (ov * _rms_scale(ov) * wn) * sig * (1.0 + zv * (1.0 - sig))).astype(BF16)
            do, dwt = _rms_bwd(ov, wn, g * zv * sig)
            do_ref[:, c] = do
            dwn_ref[...] += jnp.sum(dwt, axis=0, keepdims=True)

        def step(t, dstates):
            ci = chunks - 1 - t
            rows = pl.ds(pl.multiple_of(ci * CHUNK, CHUNK), CHUNK)
            cols = [_head_cols(hh) for hh in heads]
            state = [st_ref[hh, ci] for hh in heads]
            dov = [do_ref[rows, cols[hh]] for hh in heads]
            wv = [w_ref[rows, cols[hh]] for hh in heads]
            ws = [_dot(wv[hh], state[hh]) for hh in heads]
            adov = [_dot(a_ref[hh, rows, :], dov[hh], 0, 0) for hh in heads]
            kds = [_dot(kd_ref[rows, cols[hh]], dstates[hh]) for hh in heads]
            dqd = [_dot(dov[hh], state[hh], 1, 1) for hh in heads]
            qdo = [_dot(qd_ref[rows, cols[hh]], dov[hh], 0, 0) for hh in heads]
            vn = [u_ref[rows, cols[hh]] - ws[hh] for hh in heads]
            dvn = [adov[hh] + kds[hh] for hh in heads]
            da = [_dot(dov[hh], vn[hh], 1, 1) for hh in heads]
            dkd = [_dot(vn[hh], dstates[hh], 1, 1) for hh in heads]
            dwv = [_dot(dvn[hh], state[hh], 1, 1) for hh in heads]
            wdv = [_dot(wv[hh], dvn[hh], 0, 0) for hh in heads]
            for hh in heads:
                da_ref[hh, rows, :] = da[hh]
                dqd_ref[rows, cols[hh]] = dqd[hh]
                dkd_ref[rows, cols[hh]] = dkd[hh]
                dgl = jnp.sum(jnp.sum(dstates[hh] * state[hh], axis=1, keepdims=True), axis=0, keepdims=True)
                dgl_ref[hh, ci] = jnp.broadcast_to(dgl, (1, LANES))
                du_ref[rows, cols[hh]] = dvn[hh]
                dw_ref[rows, cols[hh]] = -dwv[hh]
            return tuple(dstates[hh] * gl_ref[hh, ci] + qdo[hh] - wdv[hh] for hh in heads)

        last = lax.fori_loop(0, chunks, step, tuple(carry_ref[hh] for hh in heads))
        for hh in heads:
            carry_ref[hh] = last[hh]

        if exchange is not None:
            @pl.when((pl.program_id(0) == groups - 1) & (pl.program_id(1) == SCAN_PARTS_BWD - 1))
            def _():
                exchange.finish(*exchange.split(ex_refs))

    hs, a_s, gl_s, st_s, gz_s, mix_s = _scan_specs(SCAN_HEADS_BWD, SCAN_PARTS_BWD, True)
    vec = pl.BlockSpec((1, GHD), lambda g, p: (0, 0))
    tok = jax.ShapeDtypeStruct((S, DGDN), F32)
    res = pl.pallas_call(
        body, name="gdn_scan_bwd", grid=(groups, SCAN_PARTS_BWD),
        in_specs=[mix_s, hs, gz_s, vec, hs, hs, hs, hs, a_s, gl_s, st_s, pl.BlockSpec(memory_space=pl.ANY)] + ex_in_specs,
        out_specs=[gz_s, hs, hs, hs, hs, a_s, gl_s, vec] + ex_out_specs,
        out_shape=[jax.ShapeDtypeStruct((S, DPROJ_PAD), BF16), tok, tok, tok, tok,
                   jax.ShapeDtypeStruct((NGH, S, CHUNK), F32), jax.ShapeDtypeStruct((NGH, NCH, 1, LANES), F32),
                   jax.ShapeDtypeStruct((1, GHD), F32)] + ex_out_shape,
        input_output_aliases={11: 0},
        scratch_shapes=[pltpu.VMEM((S // SCAN_PARTS_BWD, SCAN_HEADS_BWD * GHD), F32),
                        pltpu.VMEM((SCAN_HEADS_BWD, GHD, GHD), F32)] + ex_scratch,
        compiler_params=_cparams(),
    )(dmix, o, proj, w_norm, u, w, qd, kd, a, gl, states, dproj, *ex_in)
    return res[:8], res[8:]


def _place():
    return lax.axis_index("x"), lax.axis_index("y"), lax.axis_index("c")


def _other_chips(x, y):
    return [(1 - x, y), (x, 1 - y), (1 - x, 1 - y)]


HBM = pl.BlockSpec(memory_space=pltpu.HBM)
VMEM = pl.BlockSpec(memory_space=pltpu.VMEM)


def _half_rows(ref_or_rows, half):
    rows = ref_or_rows // 2
    return pl.ds(pl.multiple_of(half * rows, rows), rows)


class _Exchange:
    def __init__(self, inputs, out_shape, n_sems, start, finish=None, middle=None, rest=None):
        self.inputs, self.out_shape, self.n_sems, self.start = inputs, out_shape, n_sems, start
        if finish is None:
            def finish(*refs):
                middle(*refs)
                rest(*refs)
        self.finish = finish
        self.middle = middle if middle is not None else (lambda *refs: None)
        self.rest = rest if rest is not None else finish

    def sem_shapes(self):
        return [pltpu.SemaphoreType.DMA((self.n_sems,)), pltpu.SemaphoreType.DMA((self.n_sems,))]

    def split(self, refs):
        n_in, n_out = len(self.inputs), len(self.out_shape)
        return refs[:n_in], refs[n_in:n_in + n_out], refs[n_in + n_out], refs[n_in + n_out + 1]


def _allgather_exchange(shards, whole=()):
    n, nw = len(shards), len(whole)
    slots = 8

    def plan(src, outs, send_sems, recv_sems):
        x, y, c = _place()
        via_x, via_y, diagonal = _other_chips(x, y)
        id_x, id_y, id_diagonal = [2 * chip[0] + chip[1] for chip in (via_x, via_y, diagonal)]
        me, sibling = (x, y, c), (x, y, 1 - c)

        def rows_of(a, half, quarter):
            total = src[a].shape[0]
            if quarter is None:
                return _half_rows(total, half)
            return pl.ds(pl.multiple_of(half * (total // 2) + quarter * (total // 4), total // 4), total // 4)

        def copy(a, k, chip_index, half, quarter, to, from_src=False):
            rows = rows_of(a, half, quarter)
            dst = outs[a].at[chip_index, rows]
            return pltpu.make_async_remote_copy(
                src_ref=src[a].at[rows] if from_src else dst, dst_ref=dst, send_sem=send_sems.at[slots * a + k],
                recv_sem=recv_sems.at[slots * a + k], device_id=to, device_id_type=MESH)

        def whole_copy(b, k, chip_index, to):
            return pltpu.make_async_remote_copy(
                src_ref=src[n + b], dst_ref=outs[n + b].at[chip_index], send_sem=send_sems.at[slots * n + 3 * b + k],
                recv_sem=recv_sems.at[slots * n + 3 * b + k], device_id=to, device_id_type=MESH)

        first, stages, last = [], [], []
        for a in range(n):
            first += [copy(a, 0, 2 * x + y, c, None, (*via_x, c), True), copy(a, 1, 2 * x + y, c, None, (*via_y, c), True)]
            stages.append([
                (copy(a, 0, id_x, c, None, me),
                 [copy(a, 2, id_x, c, 0, (*via_y, c)), copy(a, 4, id_x, c, None, sibling)]),
                (copy(a, 1, id_y, c, None, me),
                 [copy(a, 3, id_y, c, 1, (*via_x, c)), copy(a, 5, id_y, c, None, sibling)]),
                (copy(a, 2, id_diagonal, c, 0, me), [copy(a, 6, id_diagonal, c, 0, sibling)]),
                (copy(a, 3, id_diagonal, c, 1, me), [copy(a, 7, id_diagonal, c, 1, sibling)]),
            ])
            last += [copy(a, 4, id_x, 1 - c, None, me), copy(a, 5, id_y, 1 - c, None, me),
                     copy(a, 6, id_diagonal, 1 - c, 0, me), copy(a, 7, id_diagonal, 1 - c, 1, me)]
        for b in range(nw):
            for k, (chip, index) in enumerate(((via_x, id_x), (via_y, id_y), (diagonal, id_diagonal))):
                first.append(whole_copy(b, k, 2 * x + y, (*chip, c)))
                last.append(whole_copy(b, k, index, me))
        return first, stages, last

    def start(*refs):
        for cp in plan(*refs)[0]:
            cp.start()

    def pass_on(stages, which):
        for stage in which:
            for per_shard in stages:
                lands, onward = per_shard[stage]
                lands.wait_recv()
                for cp in onward:
                    cp.start()

    def middle(*refs):
        pass_on(plan(*refs)[1], (0, 1))

    def rest(*refs):
        first, stages, last = plan(*refs)
        pass_on(stages, (2, 3))
        for cp in last:
            cp.wait_recv()
        for cp in first + [cp for per_shard in stages for _, onward in per_shard for cp in onward]:
            cp.wait_send()

    out_shape = [jax.ShapeDtypeStruct((NCHIP,) + s.shape, s.dtype) for s in list(shards) + list(whole)]
    return _Exchange(list(shards) + list(whole), out_shape, slots * n + 3 * nw, start, middle=middle, rest=rest)


def _with_own(gathered, own):
    x, y, _ = _place()
    return lax.dynamic_update_index_in_dim(gathered, own, 2 * x + y, axis=0)


def _simple_exchange(inputs, out_shape, copies_of):
    def start(*refs):
        for cp in copies_of(*refs):
            cp.start()

    def finish(*refs):
        for cp in copies_of(*refs):
            cp.wait()

    return _Exchange(list(inputs), out_shape, len(out_shape) * 3, start, finish)


def _grad_pair_sum(a, b, *, by_columns, name):
    m, n = a.shape[1], b.shape[1]
    rows, cols = (m, n // NCHIP) if by_columns else (m // NCHIP, n)
    half = rows // 2
    lag = 2

    def body(a_ref, b_ref, out_ref, mine_ref, send_ref, recv_ref, send_sems, recv_sems):
        i = pl.program_id(0)
        x, y, c = _place()

        def share(t):
            return pltpu.make_async_remote_copy(
                src_ref=send_ref.at[t], dst_ref=recv_ref.at[t], send_sem=send_sems.at[t], recv_sem=recv_sems.at[t],
                device_id=(x, y, 1 - c), device_id_type=MESH)

        def half_block(core):
            return _dot(a_ref[:, pl.ds(pl.multiple_of(core * half, half), half)], b_ref[...], 0, 0).astype(BF16)

        def block_and_send():
            mine_ref[i] = half_block(c)
            send_ref[i] = half_block(1 - c)
            share(i).start()

        def add_received():
            share(i - lag).wait_recv()
            out_ref[...] = (mine_ref[i - lag].astype(F32) + recv_ref[i - lag].astype(F32)).astype(BF16)

        @pl.when(i < lag)
        def _():
            block_and_send()

        @pl.when((i >= lag) & (i < NCHIP))
        def _():
            add_received()
            block_and_send()

        @pl.when(i >= NCHIP)
        def _():
            add_received()

        @pl.when(i == NCHIP + lag - 1)
        def _():
            for t in range(NCHIP):
                share(t).wait_send()

    def block_of(i):
        return jnp.minimum(i, NCHIP - 1)

    if by_columns:
        in_specs = [pl.BlockSpec((S, m), lambda i: (0, 0)), pl.BlockSpec((S, cols), lambda i: (0, block_of(i)))]
    else:
        in_specs = [pl.BlockSpec((S, rows), lambda i: (0, block_of(i))), pl.BlockSpec((S, n), lambda i: (0, 0))]
    return pl.pallas_call(
        body, name=name, grid=(NCHIP + lag,), in_specs=in_specs,
        out_specs=pl.BlockSpec((None, half, cols), lambda i: (jnp.maximum(i - lag, 0), 0, 0)),
        out_shape=jax.ShapeDtypeStruct((NCHIP, half, cols), BF16),
        scratch_shapes=[pltpu.VMEM((NCHIP, half, cols), BF16), pltpu.VMEM((NCHIP, half, cols), BF16),
                        pltpu.VMEM((NCHIP, half, cols), BF16), pltpu.SemaphoreType.DMA((NCHIP,)),
                        pltpu.SemaphoreType.DMA((NCHIP,))],
        compiler_params=_cparams(),
    )(a, b)


def _chip_exchange(parts):
    def copies_of(src, outs, send_sems, recv_sems):
        x, y, c = _place()
        return [pltpu.make_async_remote_copy(
            src_ref=src[a].at[2 * chip[0] + chip[1]], dst_ref=outs[a].at[k], send_sem=send_sems.at[3 * a + k],
            recv_sem=recv_sems.at[3 * a + k], device_id=(*chip, c), device_id_type=MESH)
            for a in range(len(src)) for k, chip in enumerate(_other_chips(x, y))]

    return _simple_exchange(parts, [jax.ShapeDtypeStruct((NCHIP - 1,) + p.shape[1:], p.dtype) for p in parts], copies_of)


def _chip_sum(parts, received, exchange=None):
    n = len(parts)
    steps = 4
    ex_in, ex_in_specs, ex_out_specs, ex_out_shape, ex_scratch = _hosted(exchange)
    n_ex = len(ex_in)

    def body(*refs):
        mine, theirs = refs[2 * n + n_ex:3 * n + n_ex], refs[3 * n + n_ex:4 * n + n_ex]
        ex_refs = refs[2 * n:2 * n + n_ex] + refs[4 * n + n_ex:len(refs) - n - 2]
        tiles, send_sems, recv_sems = refs[len(refs) - n - 2:len(refs) - 2], refs[-2], refs[-1]
        step = pl.program_id(0)
        if exchange is not None:
            @pl.when(step == 0)
            def _():
                exchange.start(*exchange.split(ex_refs))

        x, y, c = _place()

        def share(a, i):
            rows = tiles[a].shape[1]
            return pltpu.make_async_remote_copy(
                src_ref=tiles[a].at[i], dst_ref=theirs[a].at[pl.ds(pl.multiple_of(i * rows, rows), rows)],
                send_sem=send_sems.at[a * steps + i], recv_sem=recv_sems.at[a * steps + i], device_id=(x, y, 1 - c),
                device_id_type=MESH)

        chip = 2 * x + y
        for a in range(n):
            p, r = refs[a], refs[n + a]
            own = jnp.where(chip == 0, p[0], jnp.where(chip == 1, p[1], jnp.where(chip == 2, p[2], p[3])))
            total = ((own.astype(F32) + r[0].astype(F32)) + r[1].astype(F32)) + r[2].astype(F32)
            mine[a][...] = total
            tiles[a][step] = total
            share(a, step).start()

        @pl.when(step == steps - 1)
        def _():
            if exchange is not None:
                exchange.finish(*exchange.split(ex_refs))
            for a in range(n):
                for i in range(steps):
                    share(a, i).wait()

    def specs(arrs):
        return [pl.BlockSpec((g.shape[0], g.shape[1] // steps, g.shape[2]), lambda i: (0, i, 0)) for g in arrs]

    out_specs = [pl.BlockSpec((g.shape[1] // steps, g.shape[2]), lambda i: (i, 0)) for g in parts]
    halves = [jax.ShapeDtypeStruct(g.shape[1:], F32) for g in parts]
    res = pl.pallas_call(
        body, name="grads_chip_sum", grid=(steps,), in_specs=specs(parts) + specs(received) + ex_in_specs,
        out_specs=out_specs + [HBM] * n + ex_out_specs, out_shape=halves * 2 + ex_out_shape,
        scratch_shapes=ex_scratch + [pltpu.VMEM((steps, g.shape[1] // steps, g.shape[2]), F32) for g in parts]
        + [pltpu.SemaphoreType.DMA((n * steps,))] * 2, compiler_params=_cparams(),
    )(*parts, *received, *ex_in)
    return res[:n], res[n:2 * n], res[2 * n:]


def _adamw_math(w, g, m, v):
    nm = ADAM_B1 * m + (1.0 - ADAM_B1) * g
    nv = ADAM_B2 * v + (1.0 - ADAM_B2) * jnp.square(g)
    m_hat = nm / (1.0 - ADAM_B1 ** ADAM_STEP)
    v_hat = nv / (1.0 - ADAM_B2 ** ADAM_STEP)
    return -ADAM_LR * (m_hat / (jnp.sqrt(v_hat) + ADAM_EPS) + ADAM_WD * w), nm, nv


def _adamw_big(ws, g_mine, g_theirs, ms, vs, exchange=None):
    n = len(ws)
    steps = 8
    ex_in, ex_in_specs, ex_out_specs, ex_out_shape, ex_scratch = _hosted(exchange)

    def body(*refs):
        ex_refs = refs[5 * n:5 * n + len(ex_in)] + refs[9 * n + len(ex_in):]
        outs = refs[5 * n + len(ex_in):9 * n + len(ex_in)]
        if exchange is not None:
            @pl.when(pl.program_id(0) == 0)
            def _():
                exchange.start(*exchange.split(ex_refs))

        own_half = (pl.program_id(0) // (steps // 2)) == lax.axis_index("c")
        for a in range(n):
            g = jnp.where(own_half, refs[n + a][...], refs[2 * n + a][...])
            d, nm, nv = _adamw_math(refs[a][...], g, refs[3 * n + a][...], refs[4 * n + a][...])
            outs[a][...] = g
            outs[n + a][...] = d
            outs[2 * n + a][...] = nm
            outs[3 * n + a][...] = nv

        if exchange is not None:
            @pl.when(pl.program_id(0) == steps - 1)
            def _():
                exchange.finish(*exchange.split(ex_refs))

    specs = [pl.BlockSpec((w.shape[0] // steps, w.shape[1]), lambda i: (i, 0)) for w in ws]
    half_specs = [pl.BlockSpec((g.shape[0] // (steps // 2), g.shape[1]), lambda i: (i % (steps // 2), 0)) for g in g_mine]
    shapes = [jax.ShapeDtypeStruct(w.shape, F32) for w in ws]
    res = pl.pallas_call(
        body, name="adamw_big", grid=(steps,), in_specs=specs + half_specs * 2 + specs * 2 + ex_in_specs,
        out_specs=specs * 4 + ex_out_specs, out_shape=shapes * 4 + ex_out_shape, scratch_shapes=ex_scratch,
        compiler_params=_cparams(),
    )(*ws, *g_mine, *g_theirs, *ms, *vs, *ex_in)
    return res[:n], res[n:2 * n], res[2 * n:3 * n], res[3 * n:4 * n], res[4 * n:]


def _adamw_in(w, g_mine, g_theirs, m, v):
    half = D // 2

    def body(w_ref, gm_ref, gt_ref, m_ref, v_ref, g_out, d_out, nm_out, nv_out, g_ref):
        south = lax.axis_index("c") == 0
        g_ref[0:half, :] = jnp.where(south, gm_ref[...], gt_ref[...])
        g_ref[half:D, :] = jnp.where(south, gt_ref[...], gm_ref[...])
        g = g_ref[0:CW, :]
        d, nm, nv = _adamw_math(w_ref[...], g, m_ref[...], v_ref[...])
        g_out[...] = g
        d_out[...] = d
        nm_out[...] = nm
        nv_out[...] = nv

    spec = pl.BlockSpec((CW, LANES), lambda i: (0, i))
    half_spec = pl.BlockSpec((half, LANES), lambda i: (0, i))
    return pl.pallas_call(
        body, name="adamw_in", grid=(D // LANES,), in_specs=[spec, half_spec, half_spec, spec, spec], out_specs=[spec] * 4,
        out_shape=[jax.ShapeDtypeStruct((CW, D), F32)] * 4, scratch_shapes=[pltpu.VMEM((D, LANES), F32)],
        compiler_params=_cparams(),
    )(w, g_mine, g_theirs, m, v)


NORM_NAMES = ("pre_mix_norm", "post_mix_norm", "pre_mlp_norm", "post_mlp_norm")
SMALL_NAMES = NORM_NAMES + ("gdn_conv_w", "fox_f_bias", "gdn_dt_bias", "gdn_a_log", "fox_out_norm", "gdn_out_norm")
CONV_COLS = 3 * DGDN // NCHIP


def _small_gather(d_norms, d_conv, sums, d_fox_norm, d_gdn_norm, loss_row):
    n_arrays = 6
    n_remote = n_arrays * (NDEV - 1)

    def copies_of(src, outs, send_sems, recv_sems):
        x, y, c = _place()
        me = 4 * x + 2 * y + c

        def from_me(chip_index):
            cols = pl.ds(pl.multiple_of(chip_index * CONV_COLS, LANES), CONV_COLS)
            return [src[0], src[1].at[:, cols], src[2], src[3], src[4], src[5]]

        local = [pltpu.make_async_copy(s, outs[a].at[me], send_sems.at[n_remote + a]) for a, s in enumerate(from_me(2 * x + y))]
        remote = []
        for k in range(1, NDEV):
            px, py, pc = x ^ ((k >> 2) & 1), y ^ ((k >> 1) & 1), c ^ (k & 1)
            remote += [pltpu.make_async_remote_copy(
                src_ref=s, dst_ref=outs[a].at[me], send_sem=send_sems.at[n_arrays * (k - 1) + a],
                recv_sem=recv_sems.at[n_arrays * (k - 1) + a], device_id=(px, py, pc), device_id_type=MESH)
                for a, s in enumerate(from_me(2 * px + py))]
        return local + remote

    def start(*refs):
        for cp in copies_of(*refs):
            cp.start()

    def finish(*refs):
        for cp in copies_of(*refs):
            cp.wait()

    shapes = [(4, D), (CONV_K, CONV_COLS), (8, LANES), (1, LANES), (1, LANES), (1, LANES)]
    return _Exchange([d_norms, d_conv, sums, d_fox_norm, d_gdn_norm, loss_row],
                     [jax.ShapeDtypeStruct((NDEV,) + s, F32) for s in shapes], n_remote + n_arrays, start, finish)


def _small_adamw(gathered, ws, ms, vs):
    n = len(SMALL_NAMES)
    ng = len(gathered)

    def body(*refs):
        def total(buf):
            acc = buf[0]
            for i in range(1, NDEV):
                acc = acc + buf[i]
            return acc

        t_norms, t_conv, t_sums, t_fn, t_gn, t_loss = [total(r) for r in refs[:ng]]
        w_refs, m_refs, v_refs = refs[ng:ng + n], refs[ng + n:ng + 2 * n], refs[ng + 2 * n:ng + 3 * n]
        outs = refs[ng + 3 * n:]
        outs[4 * n][...] = t_loss
        grads = [t_norms[i:i + 1, :] for i in range(4)] + [
            t_conv, t_sums[0:1, 0:NFH], t_sums[1:2, 0:NGH], t_sums[2:3, 0:NGH], t_fn[:, 0:FHD], t_gn]
        for a in range(n):
            d, nm, nv = _adamw_math(w_refs[a][...], grads[a], m_refs[a][...], v_refs[a][...])
            outs[a][...] = grads[a]
            outs[n + a][...] = d
            outs[2 * n + a][...] = nm
            outs[3 * n + a][...] = nv

    def whole(arr):
        return pl.BlockSpec(arr.shape, lambda i: (0,) * arr.ndim)

    res = pl.pallas_call(
        body, name="small_adamw", grid=(1,), in_specs=[whole(t) for t in gathered] + [whole(w) for w in ws] * 3,
        out_specs=[whole(w) for w in ws] * 4 + [pl.BlockSpec((1, LANES), lambda i: (0, 0))],
        out_shape=[jax.ShapeDtypeStruct(w.shape, F32) for w in ws] * 4 + [jax.ShapeDtypeStruct((1, LANES), F32)],
        compiler_params=_cparams(),
    )(*gathered, *ws, *ms, *vs)
    return res[:n], res[n:2 * n], res[2 * n:3 * n], res[3 * n:4 * n], res[4 * n]


CW = DPROJ // NCHIP
PROJ_RUNS = tuple((part * DFOX + hp * LANES, part * DFOX + (hp + 1) * LANES, (3 * hp + part) * LANES)
                  for hp in range(NPAIR) for part in range(3)) + (
    (1536, 1544, BLK_SMALL * LANES), (1544, 3080, BLK_GDN * LANES), (3080, 3088, BLK_SMALL * LANES + 8),
    (3088, 3600, BLK_GZ * LANES))


def _proj_pieces():
    pieces = []
    for lo, hi, at in PROJ_RUNS:
        while lo < hi:
            j = lo // CW
            end = min(hi, (j + 1) * CW)
            pieces.append((j, lo - j * CW, at, end - lo))
            at, lo = at + end - lo, end
    return pieces


RT = 256


def _to_padded_rows(gathered):
    def body(src_ref, out_ref, blocks_ref, rows_ref):
        blocks_ref[...] = src_ref[...].astype(F32)
        rows_ref[...] = jnp.zeros_like(rows_ref)
        for j, start, at, n in _proj_pieces():
            rows_ref[at:at + n, :] = blocks_ref[j, start:start + n, :]
        out_ref[...] = rows_ref[...].astype(out_ref.dtype)

    return pl.pallas_call(
        body, name="proj_rows_in", grid=(D // RT,), in_specs=[pl.BlockSpec((NCHIP, D, RT), lambda i: (0, 0, i))],
        out_specs=pl.BlockSpec((DPROJ_PAD, RT), lambda i: (0, i)), out_shape=jax.ShapeDtypeStruct((DPROJ_PAD, D), gathered.dtype),
        scratch_shapes=[pltpu.VMEM((NCHIP, D, RT), F32), pltpu.VMEM((DPROJ_PAD, RT), F32)], compiler_params=_cparams(),
    )(gathered)


def _from_padded_rows_pair_sum(w):
    steps = D // RT
    half = D // 2

    def body(src_ref, out_ref, rows_ref, blocks_ref, mine_ref, send_ref, recv_ref, send_sems, recv_sems):
        i = pl.program_id(0)
        x, y, c = _place()

        def share(t):
            return pltpu.make_async_remote_copy(
                src_ref=send_ref.at[t], dst_ref=recv_ref.at[t], send_sem=send_sems.at[t], recv_sem=recv_sems.at[t],
                device_id=(x, y, 1 - c), device_id_type=MESH)

        def rows_of(core):
            return blocks_ref[:, pl.ds(pl.multiple_of(core * half, half), half), :]

        @pl.when(i < steps)
        def _():
            rows_ref[...] = src_ref[...].astype(F32)
            blocks_ref[...] = jnp.zeros_like(blocks_ref)
            for j, start, at, n in _proj_pieces():
                blocks_ref[j, start:start + n, :] = rows_ref[at:at + n, :]
            mine_ref[i % 2] = rows_of(c)
            send_ref[i] = rows_of(1 - c).astype(BF16)
            share(i).start()

        @pl.when(i > 0)
        def _():
            share(i - 1).wait_recv()
            out_ref[...] = (mine_ref[(i - 1) % 2] + recv_ref[i - 1].astype(F32)).astype(BF16)

        @pl.when(i == steps)
        def _():
            for t in range(steps):
                share(t).wait_send()

    tile = (NCHIP, half, RT)
    return pl.pallas_call(
        body, name="proj_rows_out_pair_sum", grid=(steps + 1,),
        in_specs=[pl.BlockSpec((DPROJ_PAD, RT), lambda i: (0, jnp.minimum(i, steps - 1)))],
        out_specs=pl.BlockSpec(tile, lambda i: (0, 0, jnp.maximum(i - 1, 0))),
        out_shape=jax.ShapeDtypeStruct((NCHIP, half, D), BF16),
        scratch_shapes=[pltpu.VMEM((DPROJ_PAD, RT), F32), pltpu.VMEM((NCHIP, D, RT), F32), pltpu.VMEM((2,) + tile, F32),
                        pltpu.VMEM((steps,) + tile, BF16), pltpu.VMEM((steps,) + tile, BF16),
                        pltpu.SemaphoreType.DMA((steps,)), pltpu.SemaphoreType.DMA((steps,))],
        compiler_params=_cparams(),
    )(w)


def _local_step(x, target, first_weights, late_weights, grad_matmul, reduce_late, reduce_in, pre_mix_norm, fox_f_bias,
                fox_out_norm, gdn_a_log, gdn_dt_bias, gdn_out_norm, post_mix_norm, pre_mlp_norm, post_mlp_norm):
    bias_vec = jnp.zeros((1, LANES), F32).at[0, 0:NFH].set(fox_f_bias).at[0, LANE_G:LANE_G + NGH].set(gdn_dt_bias)
    alog_vec = jnp.zeros((1, LANES), F32).at[0, LANE_G:LANE_G + NGH].set(gdn_a_log)
    w2 = jnp.concatenate([fox_out_norm, fox_out_norm], axis=1)

    h, first = _pre_norm(x, pre_mix_norm, exchange=first_weights[0])
    win_p, conv_w = first_weights[1](first)
    proj = _matmul(h, win_p, tb=True, tm=2048, tn=768, tk=1024, name="mm_proj", exchange=late_weights[0])
    proj, late_a = proj if late_weights[0] is not None else (proj, [])
    gates = _gates(proj, bias_vec, alog_vec)
    mix, fox_o, lse, late_b = _fox_fwd(proj, gates, w2, exchange=late_weights[1])
    qkv = _gdn_pre(proj, conv_w)
    (u, w, qd, kd, a_intra, gl, t_inv), _ = _gdn_prep(qkv, gates)
    wout, wup3 = late_weights[3](late_a, late_b)
    mix, gdn_raw, states = _gdn_scan(u, w, qd, kd, a_intra, gl, proj, gdn_out_norm, mix)
    def post_mix(acc, xv, w_post, w_pre_mlp):
        x1v = xv + acc * _rms_scale(acc) * w_post
        return acc, x1v, x1v * _rms_scale(x1v) * w_pre_mlp

    mixed, x1, h2 = _matmul(mix, wout, tm=512, tn=D, tk=1024, out_dtypes=(F32, F32, BF16), name="mm_out",
                            extra=(x, post_mix_norm, pre_mlp_norm), epilogue=post_mix)

    def relu2(acc):
        r = jnp.maximum(acc, 0.0)
        return r, r * r

    up_act = _matmul(h2, wup3, b3=True, tm=1024, tn=1024, tk=1024, out_dtypes=(BF16, BF16), epilogue=relu2,
                     name="mm_up", exchange=late_weights[2])
    (up_relu, act), late_c = up_act if late_weights[2] is not None else (up_act, [])
    wdown = late_weights[4](late_c)
    def loss_head(acc, x1v, tv, w):
        err = x1v + acc * _rms_scale(acc) * w - tv
        dx2v = err * (1.0 / D)
        dyv, dwt = _rms_bwd(acc, w, dx2v)
        part = 0.5 * jnp.sum(jnp.mean(err * err, axis=-1, keepdims=True), axis=0, keepdims=True)
        return dx2v, dyv, jnp.sum(dwt, axis=0, keepdims=True), jnp.broadcast_to(part, (1, D))

    dx2, dy, d_post_mlp, loss_wide = _matmul(
        act, wdown, tm=512, tn=D, tk=DFF, out_dtypes=(F32, BF16, F32, F32), extra=(x1, target, post_mlp_norm),
        epilogue=loss_head, n_sums=2, name="mm_down")
    loss_row = loss_wide[:, :LANES]

    dwdown = grad_matmul(act, dy, by_columns=False, name="mm_dwdown")

    def relu2_bwd(acc, r):
        return (acc * 2.0 * r.astype(F32),)

    dup = _matmul(dy, wdown, tb=True, tm=1024, tn=1024, tk=1024, out_dtypes=(BF16,), extra=(up_relu,), epilogue=relu2_bwd,
                  name="mm_dact")
    dwup3 = grad_matmul(h2, dup, by_columns=True, name="mm_dwup")

    def mid_bwd(acc, x1v, dx2v, mixedv, w_pre_mlp, w_post):
        dxa, dwm = _rms_bwd(x1v, w_pre_mlp, acc)
        dx1v = dx2v + dxa
        dm, dwp = _rms_bwd(mixedv, w_post, dx1v)
        return dx1v, dm, jnp.sum(dwm, axis=0, keepdims=True), jnp.sum(dwp, axis=0, keepdims=True)

    dx1, dmixed, d_pre_mlp, d_post_mix = _matmul(
        dup, wup3, tb=True, b3=True, tm=512, tn=D, tk=DFF, out_dtypes=(F32, BF16, F32, F32),
        extra=(x1, dx2, mixed, pre_mlp_norm, post_mix_norm), epilogue=mid_bwd, n_sums=2, name="mm_dh2")
    dwout = grad_matmul(mix, dmixed, by_columns=False, name="mm_dwout")
    dmix = _matmul(dmixed, wout, tb=True, tm=2048, tk=1024, name="mm_dmix")

    dfox, delta, d_fox_norm = _fox_norm_bwd(fox_o, dmix, w2)
    dproj, dcum_fox, reduced_a = _fox_bwd(proj, dfox, gates, lse, delta, exchange=reduce_late[0](dwout, dwup3, dwdown))
    (dproj, du, dw, dqd, dkd, da, dgl, d_gdn_norm), reduced_b = _gdn_scan_bwd(
        dmix, gdn_raw, proj, gdn_out_norm, u, w, qd, kd, a_intra, gl, states, dproj, exchange=reduce_late[1]())
    dqkv, dgates_gdn, reduced_c = _gdn_prep_bwd(qkv, gates, t_inv, du, dw, dqd, dkd, da, dgl, exchange=reduce_late[2]())
    reduced_late = (reduced_a, reduced_b, reduced_c)
    dproj, d_conv = _gdn_pre_bwd(proj, conv_w, dqkv, dproj)
    dproj, sums = _gates_bwd(proj, bias_vec, alog_vec, dgates_gdn, dcum_fox, dproj)

    dwin_p = _matmul(dproj, h, ta=True, tm=1280, tn=1024, tk=2048, out_dtypes=(BF16,), name="mm_dwin")
    exchange_in = reduce_in(dwin_p)
    dh = _matmul(dproj, win_p, tm=1024, tk=DPROJ_PAD, name="mm_dh", exchange=exchange_in)
    dh, reduced_in = dh if exchange_in is not None else (dh, [])
    grad_x, d_pre_mix = _pre_norm_bwd(dh, x, pre_mix_norm, dx1)

    d_norms = jnp.concatenate([d_pre_mix, d_post_mix, d_pre_mlp, d_post_mlp], axis=0)
    return grad_x, (d_norms, d_conv, sums, d_fox_norm, d_gdn_norm, loss_row), reduced_late, reduced_in


def kernel(x, pre_mix_norm, w_in, fox_f_bias, fox_out_norm, gdn_conv_w, gdn_a_log, gdn_dt_bias, gdn_out_norm, w_out, post_mix_norm, pre_mlp_norm, w_up, w_down, post_mlp_norm, loss_target, m_pre_mix_norm, m_w_in, m_fox_f_bias, m_fox_out_norm, m_gdn_conv_w, m_gdn_a_log, m_gdn_dt_bias, m_gdn_out_norm, m_w_out, m_post_mix_norm, m_pre_mlp_norm, m_w_up, m_w_down, m_post_mlp_norm, v_pre_mix_norm, v_w_in, v_fox_f_bias, v_fox_out_norm, v_gdn_conv_w, v_gdn_a_log, v_gdn_dt_bias, v_gdn_out_norm, v_w_out, v_post_mix_norm, v_pre_mlp_norm, v_w_up, v_w_down, v_post_mlp_norm):
    weights = dict(pre_mix_norm=pre_mix_norm, w_in=w_in, fox_f_bias=fox_f_bias, fox_out_norm=fox_out_norm, gdn_conv_w=gdn_conv_w,
                   gdn_a_log=gdn_a_log, gdn_dt_bias=gdn_dt_bias, gdn_out_norm=gdn_out_norm, w_out=w_out, post_mix_norm=post_mix_norm,
                   pre_mlp_norm=pre_mlp_norm, w_up=w_up, w_down=w_down, post_mlp_norm=post_mlp_norm)
    m_in = dict(pre_mix_norm=m_pre_mix_norm, w_in=m_w_in, fox_f_bias=m_fox_f_bias, fox_out_norm=m_fox_out_norm, gdn_conv_w=m_gdn_conv_w,
                gdn_a_log=m_gdn_a_log, gdn_dt_bias=m_gdn_dt_bias, gdn_out_norm=m_gdn_out_norm, w_out=m_w_out, post_mix_norm=m_post_mix_norm,
                pre_mlp_norm=m_pre_mlp_norm, w_up=m_w_up, w_down=m_w_down, post_mlp_norm=m_post_mlp_norm)
    v_in = dict(pre_mix_norm=v_pre_mix_norm, w_in=v_w_in, fox_f_bias=v_fox_f_bias, fox_out_norm=v_fox_out_norm, gdn_conv_w=v_gdn_conv_w,
                gdn_a_log=v_gdn_a_log, gdn_dt_bias=v_gdn_dt_bias, gdn_out_norm=v_gdn_out_norm, w_out=v_w_out, post_mix_norm=v_post_mix_norm,
                pre_mlp_norm=v_pre_mlp_norm, w_up=v_w_up, w_down=v_w_down, post_mlp_norm=v_post_mlp_norm)
    order_w = ("pre_mix_norm", "w_in", "fox_f_bias", "fox_out_norm", "gdn_conv_w", "gdn_a_log", "gdn_dt_bias", "gdn_out_norm", "w_out",
               "post_mix_norm", "pre_mlp_norm", "w_up", "w_down", "post_mlp_norm")
    big = ("w_in", "w_out", "w_up", "w_down")

    def row(v):
        return v if v.ndim == 2 else v.reshape(1, -1)

    win_shard = jnp.pad(w_in.T.astype(BF16), ((0, D - CW), (0, 0)))

    def resolve_first(gathered):
        win_g, conv_g = gathered
        return (_to_padded_rows(_with_own(win_g, win_shard)),
                _with_own(conv_g, gdn_conv_w).transpose(1, 0, 2).reshape(CONV_K, 3 * DGDN))

    late_shards = [weights[n].astype(BF16) for n in big[1:]]

    gathered_down = []

    def resolve_out_up(gathered_out, gathered_mlp):
        gathered_down.append(gathered_mlp[1])
        return _with_own(gathered_out[0], late_shards[0]).reshape(D, D), _with_own(gathered_mlp[0], late_shards[1])

    def resolve_down(_):
        return _with_own(gathered_down[0], late_shards[2]).reshape(DFF, D)

    pair_sums = {}

    def late_chip_exchange(dwout, dwup3, dwdown):
        pair_sums.update(w_out=dwout, w_up=dwup3, w_down=dwdown)
        return _chip_exchange([pair_sums["w_up"], pair_sums["w_down"]])

    def reduce_in(dwin_p):
        pair_sums["w_in"] = _from_padded_rows_pair_sum(dwin_p)
        return _chip_exchange([pair_sums["w_in"]])

    grad_x, small, received_late, received_in = _local_step(
        x[0], loss_target[0], (_allgather_exchange([win_shard], whole=[gdn_conv_w]), resolve_first),
        (_allgather_exchange(late_shards[:1]), _allgather_exchange(late_shards[1:]), None, resolve_out_up, resolve_down),
        _grad_pair_sum, (late_chip_exchange, lambda: None, lambda: _chip_exchange([pair_sums["w_out"]])),
        reduce_in, row(pre_mix_norm), fox_f_bias, row(fox_out_norm), gdn_a_log, gdn_dt_bias,
        row(gdn_out_norm), row(post_mix_norm), row(pre_mlp_norm), row(post_mlp_norm))
    received_mlp, _, received_out = received_late

    g_mine, g_theirs, small_gathered = _chip_sum(
        [pair_sums[n] for n in big], list(received_in[:1]) + list(received_out[:1]) + list(received_mlp[:2]),
        exchange=_small_gather(*small))

    g_big, d_big, nm_big, nv_big, _ = _adamw_big(
        [weights[n] for n in big[1:]], g_mine[1:], g_theirs[1:], [m_in[n] for n in big[1:]], [v_in[n] for n in big[1:]])
    in_t = _adamw_in(w_in.T, g_mine[0], g_theirs[0], m_w_in.T, v_w_in.T)
    g_small, d_small, nm_small, nv_small, loss_total = _small_adamw(
        small_gathered, [row(weights[n]) for n in SMALL_NAMES], [row(m_in[n]) for n in SMALL_NAMES],
        [row(v_in[n]) for n in SMALL_NAMES])

    grads, delta, new_m, new_v = {}, {}, {}, {}
    grads["w_in"], delta["w_in"], new_m["w_in"], new_v["w_in"] = [t.T for t in in_t]
    for i, n in enumerate(big[1:]):
        grads[n], delta[n], new_m[n], new_v[n] = g_big[i], d_big[i], nm_big[i], nv_big[i]
    for i, n in enumerate(SMALL_NAMES):
        shape = weights[n].shape
        grads[n], delta[n], new_m[n], new_v[n] = (g_small[i].reshape(shape), d_small[i].reshape(shape),
                                                  nm_small[i].reshape(shape), nv_small[i].reshape(shape))
    return (loss_total[0, 0], grad_x[None], *[grads[n] for n in order_w], *[delta[n] for n in order_w], *[new_m[n] for n in order_w],
            *[new_v[n] for n in order_w])
```

```python
import jax
import jax.numpy as jnp
from jax import lax
from jax.experimental import pallas as pl
from jax.experimental.pallas import tpu as pltpu

F32 = jnp.float32
BF16 = jnp.bfloat16
MESH = pl.DeviceIdType.MESH

S = 2048
D = 1024
NFH, FHD = 8, 64
NPAIR = NFH // 2
NGH, GHD = 4, 128
DFOX = NFH * FHD
DGDN = NGH * GHD
CHUNK = 64
NCH = S // CHUNK
CONV_K = 4
DFF = 4 * D
EPS = 1e-6
DPROJ = 3600
LANES = 128
DPROJ_PAD = 3840
BLK_GDN = 12
BLK_GZ = 24
BLK_SMALL = 28
NCHIP = 4
NDEV = 8
VMEM_LIMIT = 56 * 1024 * 1024

ADAM_LR = 0.001
ADAM_B1 = 0.9
ADAM_B2 = 0.999
ADAM_EPS = 1e-08
ADAM_WD = 0.01
ADAM_STEP = 10


def _cparams(**kw):
    return pltpu.CompilerParams(vmem_limit_bytes=VMEM_LIMIT, **kw)


def _dn(ca, cb):
    return (((ca,), (cb,)), ((), ()))


def _dot(a, b, ca=1, cb=0):
    return lax.dot_general(a.astype(BF16), b.astype(BF16), _dn(ca, cb), preferred_element_type=F32)


def _hdot(a, b, ca=1, cb=0):
    return lax.dot_general(a.astype(F32), b.astype(F32), _dn(ca, cb), precision=lax.Precision.HIGHEST,
                           preferred_element_type=F32)


def _dot3(a, b, ca=1, cb=0):
    a_hi, b_hi = a.astype(BF16), b.astype(BF16)
    a_lo, b_lo = (a - a_hi.astype(F32)).astype(BF16), (b - b_hi.astype(F32)).astype(BF16)
    dn = _dn(ca, cb)
    return (lax.dot_general(a_hi, b_hi, dn, preferred_element_type=F32)
            + (lax.dot_general(a_hi, b_lo, dn, preferred_element_type=F32)
               + lax.dot_general(a_lo, b_hi, dn, preferred_element_type=F32)))


@jax.custom_vjp
def _mm_nn(a, b):
    return _dot(a, b, 1, 0)


def _mm_nn_fwd(a, b):
    return _dot(a, b, 1, 0), (a, b)


def _mm_nn_bwd(res, g):
    a, b = res
    return _dot(g, b, 1, 1), _dot(a, g, 0, 0)


_mm_nn.defvjp(_mm_nn_fwd, _mm_nn_bwd)


@jax.custom_vjp
def _mm_nt(a, b):
    return _dot(a, b, 1, 1)


def _mm_nt_fwd(a, b):
    return _dot(a, b, 1, 1), (a, b)


def _mm_nt_bwd(res, g):
    a, b = res
    return _dot(g, b, 1, 0), _dot(g, a, 0, 0)


_mm_nt.defvjp(_mm_nt_fwd, _mm_nt_bwd)


@jax.custom_vjp
def _saved_inverse(m, t_inv):
    del m
    return t_inv


def _saved_inverse_fwd(m, t_inv):
    del m
    return t_inv, t_inv


def _saved_inverse_bwd(t_inv, g):
    return -_dot3(_dot3(t_inv, g, 0, 0), t_inv, 1, 1), jnp.zeros_like(t_inv)


_saved_inverse.defvjp(_saved_inverse_fwd, _saved_inverse_bwd)


def _sigmoid(z):
    return 1.0 / (1.0 + jnp.exp(-z))


def _softplus(z):
    return jnp.maximum(z, 0.0) + jnp.log(1.0 + jnp.exp(-jnp.abs(z)))


def _silu(z):
    return z * _sigmoid(z)


def _rms_scale(x):
    return lax.rsqrt(jnp.mean(x * x, axis=-1, keepdims=True) + EPS)


def _rms_bwd(x, w, g):
    r = _rms_scale(x)
    gw = g * w
    dx = r * gw - x * (r * r * r) * jnp.mean(gw * x, axis=-1, keepdims=True)
    return dx, g * x * r


def _matmul(a, b, *, name, ta=False, tb=False, tm=512, tn=512, tk=512, out_dtypes=(F32,), b3=False, o3=False,
            extra=(), epilogue=None, exchange=None, n_sums=0):
    m, k = (a.shape[1], a.shape[0]) if ta else a.shape
    if b3:
        n = b.shape[1] if tb else b.shape[0] * b.shape[2]
        kb = b.shape[0] * b.shape[2] if tb else b.shape[1]
    else:
        n, kb = (b.shape[0], b.shape[1]) if tb else (b.shape[1], b.shape[0])
    assert kb == k, (name, kb, k)
    tm, tn, tk = min(tm, m), min(tn, n), min(tk, k)
    assert m % tm == 0 and n % tn == 0 and k % tk == 0, (name, m, n, k, tm, tn, tk)
    nk = k // tk
    whole_k_blocks = b3 and tb and not ta and nk == 1 and b.shape[0] > 1
    n_extra = len(extra)
    n_out = len(out_dtypes)
    grid = (m // tm, n // tn, nk)
    ex_in, ex_in_specs, ex_out_specs, ex_out_shape, ex_scratch = _hosted(exchange)

    def body(*refs):
        a_ref, b_ref = refs[0], refs[1]
        extra_refs = refs[2:2 + n_extra]
        first_out = 2 + n_extra + len(ex_in)
        out_refs = refs[first_out:first_out + n_out]
        ex_refs = refs[2 + n_extra:first_out] + refs[first_out + n_out:first_out + n_out + len(ex_out_shape)] + refs[-2:]
        step = [pl.program_id(d) for d in range(3)]

        if exchange is not None:
            @pl.when((step[0] == 0) & (step[1] == 0) & (step[2] == 0))
            def _():
                exchange.start(*exchange.split(ex_refs))

        def finish(acc):
            outs = (acc,) if epilogue is None else epilogue(acc, *[r[...] for r in extra_refs])
            for o_ref, val in zip(out_refs[:n_out - n_sums], outs):
                o_ref[...] = val.astype(o_ref.dtype)
            for o_ref, val in zip(out_refs[n_out - n_sums:], outs[n_out - n_sums:]):
                @pl.when(step[0] == 0)
                def _(o_ref=o_ref, val=val):
                    o_ref[...] = val

                @pl.when(step[0] > 0)
                def _(o_ref=o_ref, val=val):
                    o_ref[...] += val

        if whole_k_blocks:
            width = b.shape[2]
            part = _dot(a_ref[:, 0:width], b_ref[0], 1, 1)
            for blk in range(1, b.shape[0]):
                part = part + _dot(a_ref[:, blk * width:(blk + 1) * width], b_ref[blk], 1, 1)
        else:
            part = _dot(a_ref[...], b_ref[...], 0 if ta else 1, 1 if tb else 0)
        if nk == 1:
            finish(part)
        else:
            acc_ref = refs[first_out + n_out + len(ex_out_shape)]

            @pl.when(step[2] == 0)
            def _():
                acc_ref[...] = part

            @pl.when(step[2] > 0)
            def _():
                acc_ref[...] += part

            @pl.when(step[2] == nk - 1)
            def _():
                finish(acc_ref[...])

        if exchange is not None:
            flat = (step[0] * grid[1] + step[1]) * nk + step[2]
            total = grid[0] * grid[1] * nk

            @pl.when(flat == total // 2)
            def _():
                exchange.middle(*exchange.split(ex_refs))

            @pl.when(flat == total - 1)
            def _():
                exchange.rest(*exchange.split(ex_refs))

    a_spec = pl.BlockSpec((tk, tm), lambda i, j, kk: (kk, i)) if ta else pl.BlockSpec((tm, tk), lambda i, j, kk: (i, kk))
    if whole_k_blocks:
        b_spec = pl.BlockSpec((b.shape[0], tn, b.shape[2]), lambda i, j, kk: (0, j, 0))
    elif b3 and tb:
        assert b.shape[2] == tk
        b_spec = pl.BlockSpec((None, tn, tk), lambda i, j, kk: (kk, j, 0))
    elif b3:
        assert b.shape[2] == tn
        b_spec = pl.BlockSpec((None, tk, tn), lambda i, j, kk: (j, kk, 0))
    elif tb:
        b_spec = pl.BlockSpec((tn, tk), lambda i, j, kk: (j, kk))
    else:
        b_spec = pl.BlockSpec((tk, tn), lambda i, j, kk: (kk, j))
    tile = pl.BlockSpec((tm, tn), lambda i, j, kk: (i, j))
    out_specs = [tile] * n_out
    out_shape = [jax.ShapeDtypeStruct((m, n), dt) for dt in out_dtypes]
    if o3:
        out_specs[0] = pl.BlockSpec((None, tm, tn), lambda i, j, kk: (j, i, 0))
        out_shape[0] = jax.ShapeDtypeStruct((n // tn, m, tn), out_dtypes[0])
    assert n_sums == 0 or tn == n
    for r in range(n_out - n_sums, n_out):
        out_specs[r] = pl.BlockSpec((1, tn), lambda i, j, kk: (0, 0))
        out_shape[r] = jax.ShapeDtypeStruct((1, n), out_dtypes[r])
    res = pl.pallas_call(
        body, name=name, grid=grid,
        in_specs=[a_spec, b_spec] + [tile if e.shape[0] == m else pl.BlockSpec((1, tn), lambda i, j, kk: (0, j)) for e in extra]
        + ex_in_specs, out_specs=out_specs + ex_out_specs,
        out_shape=out_shape + ex_out_shape,
        scratch_shapes=([pltpu.VMEM((tm, tn), F32)] if nk > 1 else []) + ex_scratch,
        compiler_params=_cparams(),
    )(a, b, *extra, *ex_in)
    if exchange is not None:
        return (res[0] if n_out == 1 else res[:n_out]), res[n_out:]
    return res[0] if n_out == 1 else res


TR = 256


def _row_spec(cols):
    return pl.BlockSpec((TR, cols), lambda i: (i, 0))


def _vec_spec(cols):
    return pl.BlockSpec((1, cols), lambda i: (0, 0))


def _pre_norm(x, w, exchange=None):
    ex_in, ex_in_specs, ex_out_specs, ex_out_shape, ex_scratch = _hosted(exchange)

    def body(*refs):
        x_ref, w_ref, h_ref = refs[0], refs[1], refs[2 + len(ex_in)]
        ex_refs = refs[2:2 + len(ex_in)] + refs[3 + len(ex_in):]
        if exchange is not None:
            @pl.when(pl.program_id(0) == 0)
            def _():
                exchange.start(*exchange.split(ex_refs))

        xv = x_ref[...]
        h_ref[...] = (xv * _rms_scale(xv) * w_ref[...]).astype(BF16)

        if exchange is not None:
            @pl.when(pl.program_id(0) == S // TR - 1)
            def _():
                exchange.finish(*exchange.split(ex_refs))

    res = pl.pallas_call(
        body, name="pre_norm", grid=(S // TR,), in_specs=[_row_spec(D), _vec_spec(D)] + ex_in_specs,
        out_specs=[_row_spec(D)] + ex_out_specs, out_shape=[jax.ShapeDtypeStruct((S, D), BF16)] + ex_out_shape,
        scratch_shapes=ex_scratch, compiler_params=_cparams(),
    )(x, w, *ex_in)
    return res[0], res[1:]


def _pre_norm_bwd(dh, x, w, dx1):
    def body(dh_ref, x_ref, w_ref, dx1_ref, dx_ref, dw_ref):
        i = pl.program_id(0)
        dxa, dwt = _rms_bwd(x_ref[...], w_ref[...], dh_ref[...])
        dx_ref[...] = dx1_ref[...] + dxa

        @pl.when(i == 0)
        def _():
            dw_ref[...] = jnp.zeros_like(dw_ref)

        dw_ref[...] += jnp.sum(dwt, axis=0, keepdims=True)

    return pl.pallas_call(
        body, name="pre_norm_bwd", grid=(S // TR,),
        in_specs=[_row_spec(D), _row_spec(D), _vec_spec(D), _row_spec(D)], out_specs=[_row_spec(D), _vec_spec(D)],
        out_shape=[jax.ShapeDtypeStruct((S, D), F32), jax.ShapeDtypeStruct((1, D), F32)], compiler_params=_cparams(),
    )(dh, x, w, dx1)


BQ = 512
NQ = S // BQ
LANE_BETA, LANE_G = 8, 12


def _gate_lanes(shape):
    lane = lax.broadcasted_iota(jnp.int32, shape, 1)
    return lane < LANE_BETA, (lane >= LANE_BETA) & (lane < LANE_G), (lane >= LANE_G) & (lane < LANE_G + NGH)


def _gates(proj, bias_vec, alog_vec):
    def body(s_ref, b_ref, a_ref, o_ref, carry_ref):
        i = pl.program_id(0)

        @pl.when(i == 0)
        def _():
            carry_ref[...] = jnp.zeros_like(carry_ref)

        z = s_ref[...] + b_ref[...]
        tail = jnp.log(1.0 + jnp.exp(-jnp.abs(z)))
        sp = jnp.maximum(z, 0.0) + tail
        lf = jnp.minimum(z, 0.0) - tail
        r = lax.broadcasted_iota(jnp.int32, (BQ, BQ), 0)
        c = lax.broadcasted_iota(jnp.int32, (BQ, BQ), 1)
        tri = (c <= r).astype(F32)
        cum = _hdot(tri, lf) + carry_ref[...]
        carry_ref[...] = cum[BQ - 1:BQ, :]
        is_fox, is_beta, is_g = _gate_lanes(z.shape)
        o_ref[...] = jnp.where(is_fox, cum, jnp.where(is_beta, _sigmoid(z), jnp.where(is_g, -jnp.exp(a_ref[...]) * sp, 0.0)))

    return pl.pallas_call(
        body, name="gates", grid=(NQ,),
        in_specs=[pl.BlockSpec((BQ, LANES), lambda i: (i, BLK_SMALL)), _vec_spec(LANES), _vec_spec(LANES)],
        out_specs=pl.BlockSpec((BQ, LANES), lambda i: (i, 0)), out_shape=jax.ShapeDtypeStruct((S, LANES), F32),
        scratch_shapes=[pltpu.VMEM((1, LANES), F32)], compiler_params=_cparams(),
    )(proj, bias_vec, alog_vec)


def _gates_bwd(proj, bias_vec, alog_vec, dgates_gdn, dcum_fox, dproj):
    def body(s_ref, b_ref, a_ref, dg_ref, dc_ref, dproj_in, dproj_ref, red_ref, carry_ref):
        del dproj_in
        i = pl.program_id(0)

        @pl.when(i == 0)
        def _():
            carry_ref[...] = jnp.zeros_like(carry_ref)
            red_ref[...] = jnp.zeros_like(red_ref)

        z = s_ref[...] + b_ref[...]
        dg = dg_ref[...] + dc_ref[...]
        r = lax.broadcasted_iota(jnp.int32, (BQ, BQ), 0)
        c = lax.broadcasted_iota(jnp.int32, (BQ, BQ), 1)
        upper = (c >= r).astype(F32)
        dlf = _hdot(upper, dg) + carry_ref[...]
        carry_ref[...] = dlf[0:1, :]
        sig = _sigmoid(z)
        g_scale = -jnp.exp(a_ref[...])
        is_fox, is_beta, is_g = _gate_lanes(z.shape)
        ds = jnp.where(is_fox, dlf * (1.0 - sig), jnp.where(is_beta, dg * sig * (1.0 - sig), jnp.where(is_g, dg * g_scale * sig, 0.0)))
        dproj_ref[:, 0:LANES] = ds.astype(BF16)
        dproj_ref[:, LANES:2 * LANES] = jnp.zeros((BQ, LANES), BF16)
        dalog = jnp.where(is_g, dg * g_scale * _softplus(z), 0.0)
        sums = jnp.sum(ds, axis=0, keepdims=True)
        red_ref[0:1, :] += jnp.where(is_fox[0:1], sums, 0.0)
        red_ref[1:2, :] += pltpu.roll(jnp.where(is_g[0:1], sums, 0.0), LANES - LANE_G, 1)
        red_ref[2:3, :] += pltpu.roll(jnp.sum(dalog, axis=0, keepdims=True), LANES - LANE_G, 1)

    blk = pl.BlockSpec((BQ, LANES), lambda i: (NQ - 1 - i, 0))
    return pl.pallas_call(
        body, name="gates_bwd", grid=(NQ,),
        in_specs=[pl.BlockSpec((BQ, LANES), lambda i: (NQ - 1 - i, BLK_SMALL)), _vec_spec(LANES), _vec_spec(LANES), blk, blk,
                  pl.BlockSpec(memory_space=pl.ANY)],
        out_specs=[pl.BlockSpec((BQ, 2 * LANES), lambda i: (NQ - 1 - i, BLK_SMALL // 2)), pl.BlockSpec((8, LANES), lambda i: (0, 0))],
        out_shape=[jax.ShapeDtypeStruct((S, DPROJ_PAD), BF16), jax.ShapeDtypeStruct((8, LANES), F32)],
        input_output_aliases={5: 0},
        scratch_shapes=[pltpu.VMEM((1, LANES), F32)], compiler_params=_cparams(),
    )(proj, bias_vec, alog_vec, dgates_gdn, dcum_fox, dproj)


FOX_SCALE = FHD ** -0.5
FOX_PAIRS = 2
FOX_PAIRS_BWD = 2


def _head_mask(e):
    lane = lax.broadcasted_iota(jnp.int32, (1, LANES), 1)
    return (lane >= e * FHD) & (lane < (e + 1) * FHD)


def _lane_col(vals, index):
    lane = lax.broadcasted_iota(jnp.int32, vals.shape, 1)
    return jnp.sum(jnp.where(lane == index, vals, 0.0), axis=1, keepdims=True)


def _sublane_row(vals, index):
    row = lax.broadcasted_iota(jnp.int32, vals.shape, 0)
    return jnp.sum(jnp.where(row == index, vals, 0.0), axis=0, keepdims=True)


def _pair_cols(c0, c1):
    lane = lax.broadcasted_iota(jnp.int32, (c0.shape[0], 2), 1)
    return jnp.where(lane == 0, c0, c1)


def _split3(x):
    hi = x.astype(BF16).astype(F32)
    rest = x - hi
    mid = rest.astype(BF16).astype(F32)
    return hi, mid, (rest - mid).astype(BF16).astype(F32)


def _fox_operand(vals, e, cum, is_query):
    lane = lax.broadcasted_iota(jnp.int32, (1, LANES), 1)
    base = (1 - e) * FHD
    parts = _split3(cum)
    own = jnp.where(_head_mask(e), vals * FOX_SCALE if is_query else vals, 0.0)
    cum_at, ones_at = (base, base + 3) if is_query else (base + 3, base)
    sign = 1.0 if is_query else -1.0
    out = own + jnp.where((lane >= ones_at) & (lane < ones_at + 3), 1.0, 0.0)
    for i, part in enumerate(parts):
        out = out + jnp.where(lane == cum_at + i, sign * part, 0.0)
    return out.astype(BF16)


def _causal_block():
    return lax.broadcasted_iota(jnp.int32, (BQ, BQ), 1) <= lax.broadcasted_iota(jnp.int32, (BQ, BQ), 0)


def _head_rms(o, masks):
    o2 = o * o
    r = [lax.rsqrt(jnp.sum(jnp.where(mk, o2, 0.0), axis=1, keepdims=True) * (1.0 / FHD) + EPS) for mk in masks]
    return jnp.where(masks[0], r[0], r[1])


def _hosted(exchange):
    if exchange is None:
        return [], [], [], [], []
    return (exchange.inputs, [HBM] * len(exchange.inputs), [HBM] * len(exchange.out_shape), exchange.out_shape,
            exchange.sem_shapes())


def _fox_fwd(proj, gates, w2, exchange=None):
    ex_in, ex_in_specs, ex_out_specs, ex_out_shape, ex_scratch = _hosted(exchange)

    n_in = 3 * FOX_PAIRS + 2
    heads = [(pp, e) for pp in range(FOX_PAIRS) for e in range(2)]

    def body(*refs):
        qkv_refs, g_ref, w_ref = refs[:3 * FOX_PAIRS], refs[3 * FOX_PAIRS], refs[3 * FOX_PAIRS + 1]
        mix_ref, o_ref, lse_ref = refs[n_in + len(ex_in):n_in + 3 + len(ex_in)]
        ka_ref, vb_ref = refs[n_in + 3 + len(ex_in) + len(ex_out_shape):n_in + 5 + len(ex_in) + len(ex_out_shape)]
        ex_refs = refs[n_in:n_in + len(ex_in)] + refs[n_in + 3 + len(ex_in):n_in + 3 + len(ex_in) + len(ex_out_shape)] + refs[-2:]
        grp, qi = pl.program_id(0), pl.program_id(1)

        def head_index(pp, e):
            return 2 * (FOX_PAIRS * grp + pp) + e

        if exchange is not None:
            @pl.when((grp == 0) & (qi == 0))
            def _():
                exchange.start(*exchange.split(ex_refs))

        @pl.when(qi == 0)
        def _():
            gt = g_ref[...]
            for pp in range(FOX_PAIRS):
                kv = qkv_refs[3 * pp + 1][...]
                for e in range(2):
                    ka_ref[2 * pp + e] = _fox_operand(kv, e, _lane_col(gt, head_index(pp, e)), False)
                vb_ref[pp] = qkv_refs[3 * pp + 2][...].astype(BF16)

        masks = [_head_mask(0), _head_mask(1)]
        gt = g_ref[pl.ds(pl.multiple_of(qi * BQ, BQ), BQ), :]
        qs = [_fox_operand(qkv_refs[3 * pp][...], e, _lane_col(gt, head_index(pp, e)), True) for pp, e in heads]
        n = range(len(heads))

        def block(kj, carry, diagonal):
            rows = pl.ds(pl.multiple_of(kj * BQ, BQ), BQ)
            s = [_dot(qs[i], ka_ref[i, rows, :], 1, 1) for i in n]
            if diagonal:
                s = [jnp.where(_causal_block(), s[i], -jnp.inf) for i in n]
            m_new = [jnp.maximum(carry[i][0], jnp.max(s[i], axis=-1, keepdims=True)) for i in n]
            p = [jnp.exp(s[i] - m_new[i]) for i in n]
            alpha = [jnp.exp(carry[i][0] - m_new[i]) for i in n]
            l_new = [alpha[i] * carry[i][1] + jnp.sum(p[i], axis=-1, keepdims=True) for i in n]
            pv = [_dot(p[i], vb_ref[heads[i][0], rows, :]) for i in n]
            return tuple((m_new[i], l_new[i], alpha[i] * carry[i][2] + pv[i]) for i in n)

        one = (jnp.full((BQ, 1), -jnp.inf, F32), jnp.zeros((BQ, 1), F32), jnp.zeros((BQ, LANES), F32))
        below = lax.fori_loop(0, qi, lambda kj, carry: block(kj, carry, False), (one,) * len(heads))
        done = block(qi, below, True)
        for pp in range(FOX_PAIRS):
            (m0, l0, a0), (m1, l1, a1) = done[2 * pp], done[2 * pp + 1]
            o = jnp.where(masks[0], a0 / l0, a1 / l1)
            cols = slice(pp * LANES, (pp + 1) * LANES)
            o_ref[:, cols] = o
            mix_ref[:, cols] = (o * _head_rms(o, masks) * w_ref[...]).astype(BF16)
            lse_ref[pp] = _pair_cols(m0 + jnp.log(l0), m1 + jnp.log(l1))

        if exchange is not None:
            @pl.when((grp == NPAIR // FOX_PAIRS // 2) & (qi == 0))
            def _():
                exchange.middle(*exchange.split(ex_refs))

            @pl.when((grp == NPAIR // FOX_PAIRS - 1) & (qi == NQ - 1))
            def _():
                exchange.rest(*exchange.split(ex_refs))

    qkv_specs = []
    for pp in range(FOX_PAIRS):
        qkv_specs.append(pl.BlockSpec((BQ, LANES), lambda g, i, pp=pp: (i, 3 * (FOX_PAIRS * g + pp))))
        qkv_specs.append(pl.BlockSpec((S, LANES), lambda g, i, pp=pp: (0, 3 * (FOX_PAIRS * g + pp) + 1)))
        qkv_specs.append(pl.BlockSpec((S, LANES), lambda g, i, pp=pp: (0, 3 * (FOX_PAIRS * g + pp) + 2)))
    blk = pl.BlockSpec((BQ, FOX_PAIRS * LANES), lambda g, i: (i, g))
    res = pl.pallas_call(
        body, name="fox_fwd", grid=(NPAIR // FOX_PAIRS, NQ),
        in_specs=qkv_specs + [pl.BlockSpec((S, LANES), lambda g, i: (0, 0)), pl.BlockSpec((1, LANES), lambda g, i: (0, 0))]
        + ex_in_specs,
        out_specs=[blk, blk, pl.BlockSpec((FOX_PAIRS, BQ, 2), lambda g, i: (g, i, 0))] + ex_out_specs,
        out_shape=[jax.ShapeDtypeStruct((S, D), BF16), jax.ShapeDtypeStruct((S, DFOX), F32),
                   jax.ShapeDtypeStruct((NPAIR, S, 2), F32)] + ex_out_shape,
        scratch_shapes=[pltpu.VMEM((2 * FOX_PAIRS, S, LANES), BF16), pltpu.VMEM((FOX_PAIRS, S, LANES), BF16)] + ex_scratch,
        compiler_params=_cparams(),
    )(*([proj] * (3 * FOX_PAIRS)), gates, w2, *ex_in)
    return res[0], res[1], res[2], res[3:]


def _fox_norm_bwd(o, dmix, w2, exchange=None):
    ex_in, ex_in_specs, ex_out_specs, ex_out_shape, ex_scratch = _hosted(exchange)

    def body(*refs):
        o_ref, g_ref, w_ref = refs[:3]
        do_ref, dl_ref, dw_ref = refs[3 + len(ex_in):6 + len(ex_in)]
        ex_refs = refs[3:3 + len(ex_in)] + refs[6 + len(ex_in):]
        hp, qi = pl.program_id(0), pl.program_id(1)

        if exchange is not None:
            @pl.when((hp == 0) & (qi == 0))
            def _():
                exchange.start(*exchange.split(ex_refs))

        masks = [_head_mask(0), _head_mask(1)]
        ov = o_ref[...]
        g = g_ref[...]
        r = _head_rms(ov, masks)
        gw = g * w_ref[...]
        gwo = gw * ov
        mean = [jnp.sum(jnp.where(mk, gwo, 0.0), axis=1, keepdims=True) * (1.0 / FHD) for mk in masks]
        do = r * gw - ov * (r * r * r) * jnp.where(masks[0], mean[0], mean[1])
        do_ref[...] = do.astype(BF16)
        doo = do * ov
        dl_ref[...] = _pair_cols(*[jnp.sum(jnp.where(mk, doo, 0.0), axis=1, keepdims=True) for mk in masks])

        @pl.when((hp == 0) & (qi == 0))
        def _():
            dw_ref[...] = jnp.zeros_like(dw_ref)

        dw_ref[...] += jnp.sum(g * ov * r, axis=0, keepdims=True)

        @pl.when((hp == NPAIR - 1) & (qi == NQ - 1))
        def _():
            dw = dw_ref[...]
            dw_ref[...] = dw + pltpu.roll(dw, FHD, 1)
            if exchange is not None:
                exchange.finish(*exchange.split(ex_refs))

    blk = pl.BlockSpec((BQ, LANES), lambda hp, i: (i, hp))
    vec = pl.BlockSpec((1, LANES), lambda hp, i: (0, 0))
    res = pl.pallas_call(
        body, name="fox_norm_bwd", grid=(NPAIR, NQ), in_specs=[blk, blk, vec] + ex_in_specs,
        out_specs=[blk, pl.BlockSpec((None, BQ, 2), lambda hp, i: (hp, i, 0)), vec] + ex_out_specs,
        out_shape=[jax.ShapeDtypeStruct((S, DFOX), BF16), jax.ShapeDtypeStruct((NPAIR, S, 2), F32),
                   jax.ShapeDtypeStruct((1, LANES), F32)] + ex_out_shape,
        scratch_shapes=ex_scratch, compiler_params=_cparams(),
    )(o, dmix, w2, *ex_in)
    return res[0], res[1], res[2], res[3:]


def _fox_bwd(proj, do, gates, lse, delta, exchange=None):
    ex_in, ex_in_specs, ex_out_specs, ex_out_shape, ex_scratch = _hosted(exchange)

    pg = FOX_PAIRS_BWD
    n_in = 3 * pg + 4
    heads = [(pp, e) for pp in range(pg) for e in range(2)]

    def body(*refs):
        qkv_refs = refs[:3 * pg]
        do_ref, g_ref, lse_ref, dl_ref = refs[3 * pg:n_in]
        dproj_ref, dc_ref = refs[n_in + len(ex_in):n_in + 2 + len(ex_in)]
        qa_ref, dq_ref = refs[n_in + 2 + len(ex_in) + len(ex_out_shape):n_in + 4 + len(ex_in) + len(ex_out_shape)]
        ex_refs = refs[n_in:n_in + len(ex_in)] + refs[n_in + 2 + len(ex_in):n_in + 2 + len(ex_in) + len(ex_out_shape)] + refs[-2:]
        grp, kj = pl.program_id(0), pl.program_id(1)

        def head_index(pp, e):
            return 2 * (pg * grp + pp) + e

        if exchange is not None:
            @pl.when((grp == 0) & (kj == 0))
            def _():
                exchange.start(*exchange.split(ex_refs))

        @pl.when(kj == 0)
        def _():
            gt = g_ref[...]
            for pp in range(pg):
                qv = qkv_refs[3 * pp][...]
                for e in range(2):
                    qa_ref[2 * pp + e] = _fox_operand(qv, e, _lane_col(gt, head_index(pp, e)), True)
            dq_ref[...] = jnp.zeros_like(dq_ref)

        @pl.when((grp == 0) & (kj == 0))
        def _():
            dc_ref[...] = jnp.zeros_like(dc_ref)

        masks = [_head_mask(0), _head_mask(1)]
        krows = pl.ds(pl.multiple_of(kj * BQ, BQ), BQ)
        gk = g_ref[krows, :]
        kas = [_fox_operand(qkv_refs[3 * pp + 1][...], e, _lane_col(gk, head_index(pp, e)), False) for pp, e in heads]
        vbs = [qkv_refs[3 * pp + 2][...].astype(BF16) for pp in range(pg)]
        lane = lax.broadcasted_iota(jnp.int32, (BQ, LANES), 1)
        n = range(len(heads))

        def block(qi, carry, diagonal):
            dks, dvs, css = carry
            rows = pl.ds(pl.multiple_of(qi * BQ, BQ), BQ)
            qa = [qa_ref[i, rows, :] for i in n]
            s = [_dot(qa[i], kas[i], 1, 1) for i in n]
            if diagonal:
                s = [jnp.where(_causal_block(), s[i], -jnp.inf) for i in n]
            dov = [do_ref[rows, pp * LANES:(pp + 1) * LANES] for pp in range(pg)]
            doe = [jnp.where(masks[e], dov[pp], jnp.zeros_like(dov[pp])) for pp, e in heads]
            lse2 = [lse_ref[pp, rows, :] for pp in range(pg)]
            dl2 = [dl_ref[pp, rows, :] for pp in range(pg)]
            p = [jnp.exp(s[i] - _lane_col(lse2[heads[i][0]], heads[i][1])) for i in n]
            dp = [_dot(doe[i], vbs[heads[i][0]], 1, 1) for i in n]
            ds = [p[i] * (dp[i] - _lane_col(dl2[heads[i][0]], heads[i][1])) for i in n]
            dv_part = [_dot(p[i], doe[i], 0, 0) for i in n]
            dk_part = [_dot(ds[i], jnp.where(masks[heads[i][1]], qa[i], jnp.zeros_like(qa[i])), 0, 0) for i in n]
            dq_part = [jnp.where(masks[heads[i][1]], _dot(ds[i], kas[i]), 0.0) for i in n]
            css = tuple(css[i] + jnp.sum(ds[i], axis=0, keepdims=True) for i in n)
            dc = jnp.zeros((BQ, LANES), F32)
            for i in n:
                dc = dc + jnp.where(lane == head_index(*heads[i]), jnp.sum(ds[i], axis=1, keepdims=True), 0.0)
            for pp in range(pg):
                dq_ref[pp, rows, :] += (dq_part[2 * pp] + dq_part[2 * pp + 1]) * FOX_SCALE
            dc_ref[rows, :] += dc
            dks = tuple(dks[pp] + dk_part[2 * pp] + dk_part[2 * pp + 1] for pp in range(pg))
            dvs = tuple(dvs[pp] + dv_part[2 * pp] + dv_part[2 * pp + 1] for pp in range(pg))
            return dks, dvs, css

        zero = jnp.zeros((BQ, LANES), F32)
        first = block(kj, ((zero,) * pg, (zero,) * pg, (jnp.zeros((1, BQ), F32),) * len(heads)), True)
        dks, dvs, css = lax.fori_loop(kj + 1, NQ, lambda qi, carry: block(qi, carry, False), first)
        r = lax.broadcasted_iota(jnp.int32, (BQ, BQ), 0)
        c = lax.broadcasted_iota(jnp.int32, (BQ, BQ), 1)
        dcol = jnp.zeros((BQ, LANES), F32)
        for i in n:
            col = jnp.sum(jnp.where(r == c, css[i], 0.0), axis=1, keepdims=True)
            dcol = dcol + jnp.where(lane == head_index(*heads[i]), col, 0.0)
        dc_ref[krows, :] -= dcol
        for pp in range(pg):
            base = 3 * pp * LANES
            dproj_ref[krows, base + LANES:base + 2 * LANES] = dks[pp].astype(BF16)
            dproj_ref[krows, base + 2 * LANES:base + 3 * LANES] = dvs[pp].astype(BF16)

        @pl.when(kj == NQ - 1)
        def _():
            for pp in range(pg):
                dproj_ref[:, 3 * pp * LANES:(3 * pp + 1) * LANES] = dq_ref[pp].astype(BF16)

        if exchange is not None:
            @pl.when((grp == NPAIR // pg // 2) & (kj == 0))
            def _():
                exchange.middle(*exchange.split(ex_refs))

            @pl.when((grp == NPAIR // pg - 1) & (kj == NQ - 1))
            def _():
                exchange.rest(*exchange.split(ex_refs))

    qkv_specs = []
    for pp in range(pg):
        qkv_specs.append(pl.BlockSpec((S, LANES), lambda g, j, pp=pp: (0, 3 * (pg * g + pp))))
        qkv_specs.append(pl.BlockSpec((BQ, LANES), lambda g, j, pp=pp: (j, 3 * (pg * g + pp) + 1)))
        qkv_specs.append(pl.BlockSpec((BQ, LANES), lambda g, j, pp=pp: (j, 3 * (pg * g + pp) + 2)))
    pair = pl.BlockSpec((pg, S, 2), lambda g, j: (g, 0, 0))
    res = pl.pallas_call(
        body, name="fox_bwd", grid=(NPAIR // pg, NQ),
        in_specs=qkv_specs + [pl.BlockSpec((S, pg * LANES), lambda g, j: (0, g)), pl.BlockSpec((S, LANES), lambda g, j: (0, 0)),
                              pair, pair] + ex_in_specs,
        out_specs=[pl.BlockSpec((S, 3 * pg * LANES), lambda g, j: (0, g)), pl.BlockSpec((S, LANES), lambda g, j: (0, 0))]
        + ex_out_specs,
        out_shape=[jax.ShapeDtypeStruct((S, DPROJ_PAD), BF16), jax.ShapeDtypeStruct((S, LANES), F32)] + ex_out_shape,
        scratch_shapes=[pltpu.VMEM((2 * pg, S, LANES), BF16), pltpu.VMEM((pg, S, LANES), F32)] + ex_scratch,
        compiler_params=_cparams(),
    )(*([proj] * (3 * pg)), do, gates, lse, delta, *ex_in)
    return res[0], res[1], res[2:]


NQKV = 3 * NGH
GDN_QSCALE = GHD ** -0.5


def _shift_down(x, s):
    if s == 0:
        return x
    row = lax.broadcasted_iota(jnp.int32, x.shape, 0)
    return jnp.where(row >= s, pltpu.roll(x, s, 0), 0.0)


def _shift_up(x, s):
    if s == 0:
        return x
    n = x.shape[0]
    row = lax.broadcasted_iota(jnp.int32, x.shape, 0)
    return jnp.where(row < n - s, pltpu.roll(x, n - s, 0), 0.0)


def _conv_taps(xv):
    return [_shift_down(xv, CONV_K - 1 - j) for j in range(CONV_K)]


def _conv_pre(taps, wv):
    pre = taps[CONV_K - 1] * wv[CONV_K - 1:CONV_K, :]
    for j in range(CONV_K - 1):
        pre = pre + taps[j] * wv[j:j + 1, :]
    return pre


def _l2_factors(b):
    return b < 2 * NGH, jnp.where(b < NGH, GDN_QSCALE, 1.0)


def _gdn_pre(proj, conv_w):
    def body(x_ref, w_ref, o_ref):
        b = pl.program_id(0)
        c = _silu(_conv_pre(_conv_taps(x_ref[...]), w_ref[...]))
        normed, scale = _l2_factors(b)
        rs = lax.rsqrt(jnp.sum(c * c, axis=-1, keepdims=True) + EPS)
        o_ref[...] = c * jnp.where(normed, rs, 1.0) * scale

    return pl.pallas_call(
        body, name="gdn_pre", grid=(NQKV,),
        in_specs=[pl.BlockSpec((S, GHD), lambda b: (0, BLK_GDN + b)), pl.BlockSpec((CONV_K, GHD), lambda b: (0, b))],
        out_specs=pl.BlockSpec((S, GHD), lambda b: (0, b)),
        out_shape=jax.ShapeDtypeStruct((S, NQKV * GHD), F32), compiler_params=_cparams(),
    )(proj, conv_w)


def _gdn_pre_bwd(proj, conv_w, dqkv, dproj):
    def body(x_ref, w_ref, dy_ref, dproj_in, dx_ref, dw_ref):
        del dproj_in
        b = pl.program_id(0)
        taps = _conv_taps(x_ref[...])
        wv = w_ref[...]
        pre = _conv_pre(taps, wv)
        sig = _sigmoid(pre)
        c = pre * sig
        normed, scale = _l2_factors(b)
        g = dy_ref[...] * scale
        rs = lax.rsqrt(jnp.sum(c * c, axis=-1, keepdims=True) + EPS)
        dc_n = rs * g - c * (rs * rs * rs) * jnp.sum(g * c, axis=-1, keepdims=True)
        dc = jnp.where(normed, dc_n, g)
        dpre = dc * sig * (1.0 + pre * (1.0 - sig))
        dx = dpre * wv[CONV_K - 1:CONV_K, :]
        for j in range(CONV_K - 1):
            dx = dx + _shift_up(dpre, CONV_K - 1 - j) * wv[j:j + 1, :]
        dx_ref[...] = dx.astype(BF16)
        for j in range(CONV_K):
            dw_ref[j:j + 1, :] = jnp.sum(dpre * taps[j], axis=0, keepdims=True)

    return pl.pallas_call(
        body, name="gdn_pre_bwd", grid=(NQKV,),
        in_specs=[pl.BlockSpec((S, GHD), lambda b: (0, BLK_GDN + b)), pl.BlockSpec((CONV_K, GHD), lambda b: (0, b)),
                  pl.BlockSpec((None, S, GHD), lambda b: (b // NGH, 0, b % NGH)), pl.BlockSpec(memory_space=pl.ANY)],
        out_specs=[pl.BlockSpec((S, GHD), lambda b: (0, BLK_GDN + b)), pl.BlockSpec((CONV_K, GHD), lambda b: (0, b))],
        out_shape=[jax.ShapeDtypeStruct((S, DPROJ_PAD), BF16), jax.ShapeDtypeStruct((CONV_K, NQKV * GHD), F32)],
        input_output_aliases={3: 0}, compiler_params=_cparams(),
    )(proj, conv_w, dqkv, dproj)


CB = 16
NCB = NCH // CB


def _chunk_prep(qs, ks, vs, gcols, bcols, t_saved=None):
    n = range(len(qs))
    r = lax.broadcasted_iota(jnp.int32, (CHUNK, CHUNK), 0)
    c = lax.broadcasted_iota(jnp.int32, (CHUNK, CHUNK), 1)
    incl = c <= r
    eye = (r == c).astype(F32)
    grow = [jnp.sum(gcols[i] * eye, axis=0, keepdims=True) for i in n]
    gc_col = [jnp.sum(jnp.where(incl, grow[i], 0.0), axis=1, keepdims=True) for i in n]
    gc_row = [jnp.sum(jnp.where(r <= c, gcols[i], 0.0), axis=0, keepdims=True) for i in n]
    decay = [jnp.exp(jnp.where(incl, gc_col[i] - gc_row[i], -jnp.inf)) for i in n]
    kb = [ks[i] * bcols[i] for i in n]
    vb = [vs[i] * bcols[i] for i in n]
    kk = [_mm_nt(kb[i], ks[i]) for i in n]
    m = [jnp.where(c < r, kk[i] * decay[i], 0.0) for i in n]
    if t_saved is None:
        t_inv = [eye - m[i] for i in n]
        p = [_dot3(m[i], m[i]) for i in n]
        for step in range(5):
            t_inv = [t_inv[i] + _dot3(t_inv[i], p[i]) for i in n]
            if step < 4:
                p = [_dot3(p[i], p[i]) for i in n]
    else:
        t_inv = [_saved_inverse(m[i], t_saved[i]) for i in n]
    egc = [jnp.exp(gc_col[i]) for i in n]
    u = [_mm_nn(t_inv[i], vb[i]) for i in n]
    w = [_mm_nn(t_inv[i], kb[i] * egc[i]) for i in n]
    qk = [_mm_nt(qs[i], ks[i]) for i in n]
    gc_last = [gc_col[i][CHUNK - 1:CHUNK, :] for i in n]
    return [(u[i], w[i], qk[i] * decay[i], qs[i] * egc[i], ks[i] * jnp.exp(gc_last[i] - gc_col[i]), jnp.exp(gc_last[i]),
             t_inv[i]) for i in n]


def _prep_specs():
    rows = CB * CHUNK
    qs = pl.BlockSpec((rows, GHD), lambda i, h: (i, h))
    ks = pl.BlockSpec((rows, GHD), lambda i, h: (i, NGH + h))
    vs = pl.BlockSpec((rows, GHD), lambda i, h: (i, 2 * NGH + h))
    gs = pl.BlockSpec((rows, LANES), lambda i, h: (i, 0))
    a_s = pl.BlockSpec((None, rows, CHUNK), lambda i, h: (h, i, 0))
    gl_s = pl.BlockSpec((None, CB, 1, LANES), lambda i, h: (h, i, 0, 0))
    return qs, ks, vs, gs, a_s, gl_s


def _gdn_prep(qkv, gates, exchange=None):
    ex_in, ex_in_specs, ex_out_specs, ex_out_shape, ex_scratch = _hosted(exchange)

    def body(*refs):
        q_ref, k_ref, v_ref, g_ref = refs[:4]
        u_ref, w_ref, qd_ref, kd_ref, a_ref, gl_ref, t_ref = refs[4 + len(ex_in):11 + len(ex_in)]
        ex_refs = refs[4:4 + len(ex_in)] + refs[11 + len(ex_in):]
        h = pl.program_id(1)

        if exchange is not None:
            @pl.when((pl.program_id(0) == 0) & (h == 0))
            def _():
                exchange.start(*exchange.split(ex_refs))

        chunks = [pl.ds(cidx * CHUNK, CHUNK) for cidx in range(CB)]
        gts = [g_ref[rows, :] for rows in chunks]
        outs = _chunk_prep([q_ref[rows, :] for rows in chunks], [k_ref[rows, :] for rows in chunks],
                           [v_ref[rows, :] for rows in chunks], [_lane_col(gt, LANE_G + h) for gt in gts],
                           [_lane_col(gt, LANE_BETA + h) for gt in gts])
        for cidx, rows in enumerate(chunks):
            u, w, a, qd, kd, gl, t_inv = outs[cidx]
            u_ref[rows, :] = u
            w_ref[rows, :] = w
            qd_ref[rows, :] = qd
            kd_ref[rows, :] = kd
            a_ref[rows, :] = a
            t_ref[rows, :] = t_inv
            gl_ref[cidx] = jnp.broadcast_to(gl, (1, LANES))

        if exchange is not None:
            @pl.when((pl.program_id(0) == NCB // 2) & (h == 0))
            def _():
                exchange.middle(*exchange.split(ex_refs))

            @pl.when((pl.program_id(0) == NCB - 1) & (h == NGH - 1))
            def _():
                exchange.rest(*exchange.split(ex_refs))

    qs, ks, vs, gs, a_s, gl_s = _prep_specs()
    tok = jax.ShapeDtypeStruct((S, DGDN), F32)
    sq = jax.ShapeDtypeStruct((NGH, S, CHUNK), F32)
    res = pl.pallas_call(
        body, name="gdn_prep", grid=(NCB, NGH), in_specs=[qs, ks, vs, gs] + ex_in_specs,
        out_specs=[qs, qs, qs, qs, a_s, gl_s, a_s] + ex_out_specs,
        out_shape=[tok, tok, tok, tok, sq, jax.ShapeDtypeStruct((NGH, NCH, 1, LANES), F32), sq] + ex_out_shape,
        scratch_shapes=ex_scratch, compiler_params=_cparams(),
    )(qkv, qkv, qkv, gates, *ex_in)
    return res[:7], res[7:]


def _gdn_prep_bwd(qkv, gates, t_inv, du, dw, dqd, dkd, da, dgl, exchange=None):
    ex_in, ex_in_specs, ex_out_specs, ex_out_shape, ex_scratch = _hosted(exchange)

    def body(*refs):
        q_ref, k_ref, v_ref, g_ref, t_ref, du_ref, dw_ref, dqd_ref, dkd_ref, da_ref, dgl_ref = refs[:11]
        dqkv_ref, dg_ref = refs[11 + len(ex_in):13 + len(ex_in)]
        ex_refs = refs[11:11 + len(ex_in)] + refs[13 + len(ex_in):]
        h = pl.program_id(1)

        if exchange is not None:
            @pl.when((pl.program_id(0) == 0) & (h == 0))
            def _():
                exchange.start(*exchange.split(ex_refs))

        @pl.when(h == 0)
        def _():
            dg_ref[...] = jnp.zeros_like(dg_ref)

        lane = lax.broadcasted_iota(jnp.int32, (CHUNK, LANES), 1)
        chunks = [pl.ds(cidx * CHUNK, CHUNK) for cidx in range(CB)]
        gts = [g_ref[rows, :] for rows in chunks]
        t_saved = [t_ref[rows, :] for rows in chunks]
        _, vjp = jax.vjp(lambda *args: [o[:6] for o in _chunk_prep(*args, t_saved=t_saved)],
                         [q_ref[rows, :] for rows in chunks], [k_ref[rows, :] for rows in chunks],
                         [v_ref[rows, :] for rows in chunks], [_lane_col(gt, LANE_G + h) for gt in gts],
                         [_lane_col(gt, LANE_BETA + h) for gt in gts])
        dqs, dks, dvs, dgcs, dbcs = vjp([(du_ref[rows, :], dw_ref[rows, :], da_ref[rows, :], dqd_ref[rows, :],
                                          dkd_ref[rows, :], dgl_ref[cidx][:, 0:1]) for cidx, rows in enumerate(chunks)])
        for cidx, rows in enumerate(chunks):
            dq, dk, dv, dgc, dbc = dqs[cidx], dks[cidx], dvs[cidx], dgcs[cidx], dbcs[cidx]
            dqkv_ref[0, rows, :] = dq
            dqkv_ref[1, rows, :] = dk
            dqkv_ref[2, rows, :] = dv
            dg_ref[rows, :] += jnp.where(lane == LANE_G + h, dgc, 0.0) + jnp.where(lane == LANE_BETA + h, dbc, 0.0)

        if exchange is not None:
            @pl.when((pl.program_id(0) == NCB - 1) & (h == NGH - 1))
            def _():
                exchange.finish(*exchange.split(ex_refs))

    qs, ks, vs, gs, a_s, gl_s = _prep_specs()
    res = pl.pallas_call(
        body, name="gdn_prep_bwd", grid=(NCB, NGH), in_specs=[qs, ks, vs, gs, a_s, qs, qs, qs, qs, a_s, gl_s] + ex_in_specs,
        out_specs=[pl.BlockSpec((3, CB * CHUNK, GHD), lambda i, h: (0, i, h)), gs] + ex_out_specs,
        out_shape=[jax.ShapeDtypeStruct((3, S, DGDN), F32), jax.ShapeDtypeStruct((S, LANES), F32)] + ex_out_shape,
        scratch_shapes=ex_scratch, compiler_params=_cparams(),
    )(qkv, qkv, qkv, gates, t_inv, du, dw, dqd, dkd, da, dgl, *ex_in)
    return res[0], res[1], res[2:]


def _scan_specs(nh, parts, reverse):
    wide, rows, chunks = nh * GHD, S // parts, NCH // parts

    def part(p):
        return parts - 1 - p if reverse else p

    hs = pl.BlockSpec((rows, wide), lambda g, p: (part(p), g))
    a_s = pl.BlockSpec((nh, rows, CHUNK), lambda g, p: (g, part(p), 0))
    gl_s = pl.BlockSpec((nh, chunks, 1, LANES), lambda g, p: (g, part(p), 0, 0))
    st_s = pl.BlockSpec((nh, chunks, GHD, GHD), lambda g, p: (g, part(p), 0, 0))
    gz_s = pl.BlockSpec((rows, wide), lambda g, p: (part(p), BLK_GZ // nh + g))
    mix_s = pl.BlockSpec((rows, wide), lambda g, p: (part(p), NPAIR // nh + g))
    return hs, a_s, gl_s, st_s, gz_s, mix_s


def _head_cols(hh):
    return slice(hh * GHD, (hh + 1) * GHD)


SCAN_HEADS, SCAN_PARTS = 4, 2
SCAN_HEADS_BWD, SCAN_PARTS_BWD = 4, 4


def _gdn_scan(u, w, qd, kd, a, gl, proj, w_norm, mix):
    heads = range(SCAN_HEADS)

    def body(u_ref, w_ref, qd_ref, kd_ref, a_ref, gl_ref, z_ref, wn_ref, mix_in, mix_ref, o_ref, st_ref, carry_ref):
        del mix_in

        @pl.when(pl.program_id(1) == 0)
        def _():
            carry_ref[...] = jnp.zeros_like(carry_ref)

        def step(ci, states):
            rows = pl.ds(pl.multiple_of(ci * CHUNK, CHUNK), CHUNK)
            for hh in heads:
                st_ref[hh, ci] = states[hh]
            ws = [_dot(w_ref[rows, _head_cols(hh)], states[hh]) for hh in heads]
            qs = [_dot(qd_ref[rows, _head_cols(hh)], states[hh]) for hh in heads]
            vn = [u_ref[rows, _head_cols(hh)] - ws[hh] for hh in heads]
            av = [_dot(a_ref[hh, rows, :], vn[hh]) for hh in heads]
            kv = [_dot(kd_ref[rows, _head_cols(hh)], vn[hh], 0, 0) for hh in heads]
            for hh in heads:
                o_ref[rows, _head_cols(hh)] = qs[hh] + av[hh]
            return tuple(states[hh] * gl_ref[hh, ci] + kv[hh] for hh in heads)

        last = lax.fori_loop(0, NCH // SCAN_PARTS, step, tuple(carry_ref[hh] for hh in heads))
        for hh in heads:
            carry_ref[hh] = last[hh]
            ov = o_ref[:, _head_cols(hh)]
            mix_ref[:, _head_cols(hh)] = (ov * _rms_scale(ov) * wn_ref[...] * _silu(z_ref[:, _head_cols(hh)])).astype(BF16)

    hs, a_s, gl_s, st_s, gz_s, mix_s = _scan_specs(SCAN_HEADS, SCAN_PARTS, False)
    return pl.pallas_call(
        body, name="gdn_scan", grid=(NGH // SCAN_HEADS, SCAN_PARTS),
        in_specs=[hs, hs, hs, hs, a_s, gl_s, gz_s, pl.BlockSpec((1, GHD), lambda g, p: (0, 0)),
                  pl.BlockSpec(memory_space=pl.ANY)],
        out_specs=[mix_s, hs, st_s],
        out_shape=[jax.ShapeDtypeStruct((S, D), BF16), jax.ShapeDtypeStruct((S, DGDN), F32),
                   jax.ShapeDtypeStruct((NGH, NCH, GHD, GHD), F32)],
        input_output_aliases={8: 0}, scratch_shapes=[pltpu.VMEM((SCAN_HEADS, GHD, GHD), F32)], compiler_params=_cparams(),
    )(u, w, qd, kd, a, gl, proj, w_norm, mix)


def _gdn_scan_bwd(dmix, o, proj, w_norm, u, w, qd, kd, a, gl, states, dproj, exchange=None):
    ex_in, ex_in_specs, ex_out_specs, ex_out_shape, ex_scratch = _hosted(exchange)
    groups = NGH // SCAN_HEADS_BWD

    def body(*refs):
        dy_ref, o_ref, z_ref, wn_ref, u_ref, w_ref, qd_ref, kd_ref, a_ref, gl_ref, st_ref = refs[:11]
        dz_ref, du_ref, dw_ref, dqd_ref, dkd_ref, da_ref, dgl_ref, dwn_ref = refs[12 + len(ex_in):20 + len(ex_in)]
        do_ref, carry_ref = refs[20 + len(ex_in) + len(ex_out_shape):22 + len(ex_in) + len(ex_out_shape)]
        ex_refs = refs[12:12 + len(ex_in)] + refs[20 + len(ex_in):20 + len(ex_in) + len(ex_out_shape)] + refs[-2:]
        heads = range(SCAN_HEADS_BWD)
        chunks = NCH // SCAN_PARTS_BWD

        if exchange is not None:
            @pl.when((pl.program_id(0) == 0) & (pl.program_id(1) == 0))
            def _():
                exchange.start(*exchange.split(ex_refs))

        @pl.when((pl.program_id(0) == 0) & (pl.program_id(1) == 0))
        def _():
            dwn_ref[...] = jnp.zeros_like(dwn_ref)

        @pl.when(pl.program_id(1) == 0)
        def _():
            carry_ref[...] = jnp.zeros_like(carry_ref)

        wn = wn_ref[...]
        for hh in heads:
            c = _head_cols(hh)
            ov = o_ref[:, c]
            zv = z_ref[:, c]
            g = dy_ref[:, c]
            sig = _sigmoid(zv)
            dz_ref[:, c] = (g * (ov * _rms_scale(ov) * wn) * sig * (1.0 + zv * (1.0 - sig))).astype(BF16)
            do, dwt = _rms_bwd(ov, wn, g * zv * sig)
            do_ref[:, c] = do
            dwn_ref[...] += jnp.sum(dwt, axis=0, keepdims=True)

        def step(t, dstates):
            ci = chunks - 1 - t
            rows = pl.ds(pl.multiple_of(ci * CHUNK, CHUNK), CHUNK)
            cols = [_head_cols(hh) for hh in heads]
            state = [st_ref[hh, ci] for hh in heads]
            dov = [do_ref[rows, cols[hh]] for hh in heads]
            wv = [w_ref[rows, cols[hh]] for hh in heads]
            ws = [_dot(wv[hh], state[hh]) for hh in heads]
            adov = [_dot(a_ref[hh, rows, :], dov[hh], 0, 0) for hh in heads]
            kds = [_dot(kd_ref[rows, cols[hh]], dstates[hh]) for hh in heads]
            dqd = [_dot(dov[hh], state[hh], 1, 1) for hh in heads]
            qdo = [_dot(qd_ref[rows, cols[hh]], dov[hh], 0, 0) for hh in heads]
            vn = [u_ref[rows, cols[hh]] - ws[hh] for hh in heads]
            dvn = [adov[hh] + kds[hh] for hh in heads]
            da = [_dot(dov[hh], vn[hh], 1, 1) for hh in heads]
            dkd = [_dot(vn[hh], dstates[hh], 1, 1) for hh in heads]
            dwv = [_dot(dvn[hh], state[hh], 1, 1) for hh in heads]
            wdv = [_dot(wv[hh], dvn[hh], 0, 0) for hh in heads]
            for hh in heads:
                da_ref[hh, rows, :] = da[hh]
                dqd_ref[rows, cols[hh]] = dqd[hh]
                dkd_ref[rows, cols[hh]] = dkd[hh]
                dgl = jnp.sum(jnp.sum(dstates[hh] * state[hh], axis=1, keepdims=True), axis=0, keepdims=True)
                dgl_ref[hh, ci] = jnp.broadcast_to(dgl, (1, LANES))
                du_ref[rows, cols[hh]] = dvn[hh]
                dw_ref[rows, cols[hh]] = -dwv[hh]
            return tuple(dstates[hh] * gl_ref[hh, ci] + qdo[hh] - wdv[hh] for hh in heads)

        last = lax.fori_loop(0, chunks, step, tuple(carry_ref[hh] for hh in heads))
        for hh in heads:
            carry_ref[hh] = last[hh]

        if exchange is not None:
            @pl.when((pl.program_id(0) == groups - 1) & (pl.program_id(1) == SCAN_PARTS_BWD - 1))
            def _():
                exchange.finish(*exchange.split(ex_refs))

    hs, a_s, gl_s, st_s, gz_s, mix_s = _scan_specs(SCAN_HEADS_BWD, SCAN_PARTS_BWD, True)
    vec = pl.BlockSpec((1, GHD), lambda g, p: (0, 0))
    tok = jax.ShapeDtypeStruct((S, DGDN), F32)
    res = pl.pallas_call(
        body, name="gdn_scan_bwd", grid=(groups, SCAN_PARTS_BWD),
        in_specs=[mix_s, hs, gz_s, vec, hs, hs, hs, hs, a_s, gl_s, st_s, pl.BlockSpec(memory_space=pl.ANY)] + ex_in_specs,
        out_specs=[gz_s, hs, hs, hs, hs, a_s, gl_s, vec] + ex_out_specs,
        out_shape=[jax.ShapeDtypeStruct((S, DPROJ_PAD), BF16), tok, tok, tok, tok,
                   jax.ShapeDtypeStruct((NGH, S, CHUNK), F32), jax.ShapeDtypeStruct((NGH, NCH, 1, LANES), F32),
                   jax.ShapeDtypeStruct((1, GHD), F32)] + ex_out_shape,
        input_output_aliases={11: 0},
        scratch_shapes=[pltpu.VMEM((S // SCAN_PARTS_BWD, SCAN_HEADS_BWD * GHD), F32),
                        pltpu.VMEM((SCAN_HEADS_BWD, GHD, GHD), F32)] + ex_scratch,
        compiler_params=_cparams(),
    )(dmix, o, proj, w_norm, u, w, qd, kd, a, gl, states, dproj, *ex_in)
    return res[:8], res[8:]


def _place():
    return lax.axis_index("x"), lax.axis_index("y"), lax.axis_index("c")


def _other_chips(x, y):
    return [(1 - x, y), (x, 1 - y), (1 - x, 1 - y)]


HBM = pl.BlockSpec(memory_space=pltpu.HBM)
VMEM = pl.BlockSpec(memory_space=pltpu.VMEM)


def _half_rows(ref_or_rows, half):
    rows = ref_or_rows // 2
    return pl.ds(pl.multiple_of(half * rows, rows), rows)


class _Exchange:
    def __init__(self, inputs, out_shape, n_sems, start, finish=None, middle=None, rest=None):
        self.inputs, self.out_shape, self.n_sems, self.start = inputs, out_shape, n_sems, start
        if finish is None:
            def finish(*refs):
                middle(*refs)
                rest(*refs)
        self.finish = finish
        self.middle = middle if middle is not None else (lambda *refs: None)
        self.rest = rest if rest is not None else finish

    def sem_shapes(self):
        return [pltpu.SemaphoreType.DMA((self.n_sems,)), pltpu.SemaphoreType.DMA((self.n_sems,))]

    def split(self, refs):
        n_in, n_out = len(self.inputs), len(self.out_shape)
        return refs[:n_in], refs[n_in:n_in + n_out], refs[n_in + n_out], refs[n_in + n_out + 1]


def _allgather_exchange(shards, whole=()):
    n, nw = len(shards), len(whole)
    slots = 8

    def plan(src, outs, send_sems, recv_sems):
        x, y, c = _place()
        via_x, via_y, diagonal = _other_chips(x, y)
        id_x, id_y, id_diagonal = [2 * chip[0] + chip[1] for chip in (via_x, via_y, diagonal)]
        me, sibling = (x, y, c), (x, y, 1 - c)

        def rows_of(a, half, quarter):
            total = src[a].shape[0]
            if quarter is None:
                return _half_rows(total, half)
            return pl.ds(pl.multiple_of(half * (total // 2) + quarter * (total // 4), total // 4), total // 4)

        def copy(a, k, chip_index, half, quarter, to, from_src=False):
            rows = rows_of(a, half, quarter)
            dst = outs[a].at[chip_index, rows]
            return pltpu.make_async_remote_copy(
                src_ref=src[a].at[rows] if from_src else dst, dst_ref=dst, send_sem=send_sems.at[slots * a + k],
                recv_sem=recv_sems.at[slots * a + k], device_id=to, device_id_type=MESH)

        def whole_copy(b, k, chip_index, to):
            return pltpu.make_async_remote_copy(
                src_ref=src[n + b], dst_ref=outs[n + b].at[chip_index], send_sem=send_sems.at[slots * n + 3 * b + k],
                recv_sem=recv_sems.at[slots * n + 3 * b + k], device_id=to, device_id_type=MESH)

        first, stages, last = [], [], []
        for a in range(n):
            first += [copy(a, 0, 2 * x + y, c, None, (*via_x, c), True), copy(a, 1, 2 * x + y, c, None, (*via_y, c), True)]
            stages.append([
                (copy(a, 0, id_x, c, None, me),
                 [copy(a, 2, id_x, c, 0, (*via_y, c)), copy(a, 4, id_x, c, None, sibling)]),
                (copy(a, 1, id_y, c, None, me),
                 [copy(a, 3, id_y, c, 1, (*via_x, c)), copy(a, 5, id_y, c, None, sibling)]),
                (copy(a, 2, id_diagonal, c, 0, me), [copy(a, 6, id_diagonal, c, 0, sibling)]),
                (copy(a, 3, id_diagonal, c, 1, me), [copy(a, 7, id_diagonal, c, 1, sibling)]),
            ])
            last += [copy(a, 4, id_x, 1 - c, None, me), copy(a, 5, id_y, 1 - c, None, me),
                     copy(a, 6, id_diagonal, 1 - c, 0, me), copy(a, 7, id_diagonal, 1 - c, 1, me)]
        for b in range(nw):
            for k, (chip, index) in enumerate(((via_x, id_x), (via_y, id_y), (diagonal, id_diagonal))):
                first.append(whole_copy(b, k, 2 * x + y, (*chip, c)))
                last.append(whole_copy(b, k, index, me))
        return first, stages, last

    def start(*refs):
        for cp in plan(*refs)[0]:
            cp.start()

    def pass_on(stages, which):
        for stage in which:
            for per_shard in stages:
                lands, onward = per_shard[stage]
                lands.wait_recv()
                for cp in onward:
                    cp.start()

    def middle(*refs):
        pass_on(plan(*refs)[1], (0, 1))

    def rest(*refs):
        first, stages, last = plan(*refs)
        pass_on(stages, (2, 3))
        for cp in last:
            cp.wait_recv()
        for cp in first + [cp for per_shard in stages for _, onward in per_shard for cp in onward]:
            cp.wait_send()

    out_shape = [jax.ShapeDtypeStruct((NCHIP,) + s.shape, s.dtype) for s in list(shards) + list(whole)]
    return _Exchange(list(shards) + list(whole), out_shape, slots * n + 3 * nw, start, middle=middle, rest=rest)


def _with_own(gathered, own):
    x, y, _ = _place()
    return lax.dynamic_update_index_in_dim(gathered, own, 2 * x + y, axis=0)


def _simple_exchange(inputs, out_shape, copies_of):
    def start(*refs):
        for cp in copies_of(*refs):
            cp.start()

    def finish(*refs):
        for cp in copies_of(*refs):
            cp.wait()

    return _Exchange(list(inputs), out_shape, len(out_shape) * 3, start, finish)


def _pair_exchange(grads):
    def copies_of(src, outs, send_sems, recv_sems):
        x, y, c = _place()
        return [pltpu.make_async_remote_copy(
            src_ref=src[a].at[:, _half_rows(src[a].shape[1], 1 - c)], dst_ref=outs[a], send_sem=send_sems.at[a],
            recv_sem=recv_sems.at[a], device_id=(x, y, 1 - c), device_id_type=MESH) for a in range(len(src))]

    return _simple_exchange(
        grads, [jax.ShapeDtypeStruct((g.shape[0], g.shape[1] // 2, g.shape[2]), g.dtype) for g in grads], copies_of)


def _pair_sum(grads, theirs, name):
    n = len(grads)

    def body(*refs):
        south = lax.axis_index("c") == 0
        for a in range(n):
            g = refs[a][...]
            half = g.shape[0] // 2
            mine = jnp.where(south, g[:half], g[half:])
            refs[2 * n + a][...] = (mine.astype(F32) + refs[n + a][...].astype(F32)).astype(BF16)

    def specs(arrs):
        return [pl.BlockSpec((None,) + g.shape[1:], lambda j: (j, 0, 0)) for g in arrs]

    return pl.pallas_call(
        body, name=name, grid=(NCHIP,), in_specs=specs(grads) + specs(theirs), out_specs=specs(theirs),
        out_shape=[jax.ShapeDtypeStruct(g.shape, BF16) for g in theirs], compiler_params=_cparams(),
    )(*grads, *theirs)


def _chip_exchange(parts):
    def copies_of(src, outs, send_sems, recv_sems):
        x, y, c = _place()
        return [pltpu.make_async_remote_copy(
            src_ref=src[a].at[2 * chip[0] + chip[1]], dst_ref=outs[a].at[k], send_sem=send_sems.at[3 * a + k],
            recv_sem=recv_sems.at[3 * a + k], device_id=(*chip, c), device_id_type=MESH)
            for a in range(len(src)) for k, chip in enumerate(_other_chips(x, y))]

    return _simple_exchange(parts, [jax.ShapeDtypeStruct((NCHIP - 1,) + p.shape[1:], p.dtype) for p in parts], copies_of)


def _chip_sum(parts, received, exchange=None):
    n = len(parts)
    steps = 4
    ex_in, ex_in_specs, ex_out_specs, ex_out_shape, ex_scratch = _hosted(exchange)
    n_ex = len(ex_in)

    def body(*refs):
        mine, theirs = refs[2 * n + n_ex:3 * n + n_ex], refs[3 * n + n_ex:4 * n + n_ex]
        ex_refs = refs[2 * n:2 * n + n_ex] + refs[4 * n + n_ex:len(refs) - n - 2]
        tiles, send_sems, recv_sems = refs[len(refs) - n - 2:len(refs) - 2], refs[-2], refs[-1]
        step = pl.program_id(0)
        if exchange is not None:
            @pl.when(step == 0)
            def _():
                exchange.start(*exchange.split(ex_refs))

        x, y, c = _place()

        def share(a, i):
            rows = tiles[a].shape[1]
            return pltpu.make_async_remote_copy(
                src_ref=tiles[a].at[i], dst_ref=theirs[a].at[pl.ds(pl.multiple_of(i * rows, rows), rows)],
                send_sem=send_sems.at[a * steps + i], recv_sem=recv_sems.at[a * steps + i], device_id=(x, y, 1 - c),
                device_id_type=MESH)

        chip = 2 * x + y
        for a in range(n):
            p, r = refs[a], refs[n + a]
            own = jnp.where(chip == 0, p[0], jnp.where(chip == 1, p[1], jnp.where(chip == 2, p[2], p[3])))
            total = ((own.astype(F32) + r[0].astype(F32)) + r[1].astype(F32)) + r[2].astype(F32)
            mine[a][...] = total
            tiles[a][step] = total
            share(a, step).start()

        @pl.when(step == steps - 1)
        def _():
            if exchange is not None:
                exchange.finish(*exchange.split(ex_refs))
            for a in range(n):
                for i in range(steps):
                    share(a, i).wait()

    def specs(arrs):
        return [pl.BlockSpec((g.shape[0], g.shape[1] // steps, g.shape[2]), lambda i: (0, i, 0)) for g in arrs]

    out_specs = [pl.BlockSpec((g.shape[1] // steps, g.shape[2]), lambda i: (i, 0)) for g in parts]
    halves = [jax.ShapeDtypeStruct(g.shape[1:], F32) for g in parts]
    res = pl.pallas_call(
        body, name="grads_chip_sum", grid=(steps,), in_specs=specs(parts) + specs(received) + ex_in_specs,
        out_specs=out_specs + [HBM] * n + ex_out_specs, out_shape=halves * 2 + ex_out_shape,
        scratch_shapes=ex_scratch + [pltpu.VMEM((steps, g.shape[1] // steps, g.shape[2]), F32) for g in parts]
        + [pltpu.SemaphoreType.DMA((n * steps,))] * 2, compiler_params=_cparams(),
    )(*parts, *received, *ex_in)
    return res[:n], res[n:2 * n], res[2 * n:]


def _adamw_math(w, g, m, v):
    nm = ADAM_B1 * m + (1.0 - ADAM_B1) * g
    nv = ADAM_B2 * v + (1.0 - ADAM_B2) * jnp.square(g)
    m_hat = nm / (1.0 - ADAM_B1 ** ADAM_STEP)
    v_hat = nv / (1.0 - ADAM_B2 ** ADAM_STEP)
    return -ADAM_LR * (m_hat / (jnp.sqrt(v_hat) + ADAM_EPS) + ADAM_WD * w), nm, nv


def _adamw_big(ws, g_mine, g_theirs, ms, vs, exchange=None):
    n = len(ws)
    steps = 8
    ex_in, ex_in_specs, ex_out_specs, ex_out_shape, ex_scratch = _hosted(exchange)

    def body(*refs):
        ex_refs = refs[5 * n:5 * n + len(ex_in)] + refs[9 * n + len(ex_in):]
        outs = refs[5 * n + len(ex_in):9 * n + len(ex_in)]
        if exchange is not None:
            @pl.when(pl.program_id(0) == 0)
            def _():
                exchange.start(*exchange.split(ex_refs))

        own_half = (pl.program_id(0) // (steps // 2)) == lax.axis_index("c")
        for a in range(n):
            g = jnp.where(own_half, refs[n + a][...], refs[2 * n + a][...])
            d, nm, nv = _adamw_math(refs[a][...], g, refs[3 * n + a][...], refs[4 * n + a][...])
            outs[a][...] = g
            outs[n + a][...] = d
            outs[2 * n + a][...] = nm
            outs[3 * n + a][...] = nv

        if exchange is not None:
            @pl.when(pl.program_id(0) == steps - 1)
            def _():
                exchange.finish(*exchange.split(ex_refs))

    specs = [pl.BlockSpec((w.shape[0] // steps, w.shape[1]), lambda i: (i, 0)) for w in ws]
    half_specs = [pl.BlockSpec((g.shape[0] // (steps // 2), g.shape[1]), lambda i: (i % (steps // 2), 0)) for g in g_mine]
    shapes = [jax.ShapeDtypeStruct(w.shape, F32) for w in ws]
    res = pl.pallas_call(
        body, name="adamw_big", grid=(steps,), in_specs=specs + half_specs * 2 + specs * 2 + ex_in_specs,
        out_specs=specs * 4 + ex_out_specs, out_shape=shapes * 4 + ex_out_shape, scratch_shapes=ex_scratch,
        compiler_params=_cparams(),
    )(*ws, *g_mine, *g_theirs, *ms, *vs, *ex_in)
    return res[:n], res[n:2 * n], res[2 * n:3 * n], res[3 * n:4 * n], res[4 * n:]


def _adamw_in(w, g_mine, g_theirs, m, v):
    half = D // 2

    def body(w_ref, gm_ref, gt_ref, m_ref, v_ref, g_out, d_out, nm_out, nv_out, g_ref):
        south = lax.axis_index("c") == 0
        g_ref[0:half, :] = jnp.where(south, gm_ref[...], gt_ref[...])
        g_ref[half:D, :] = jnp.where(south, gt_ref[...], gm_ref[...])
        g = g_ref[0:CW, :]
        d, nm, nv = _adamw_math(w_ref[...], g, m_ref[...], v_ref[...])
        g_out[...] = g
        d_out[...] = d
        nm_out[...] = nm
        nv_out[...] = nv

    spec = pl.BlockSpec((CW, LANES), lambda i: (0, i))
    half_spec = pl.BlockSpec((half, LANES), lambda i: (0, i))
    return pl.pallas_call(
        body, name="adamw_in", grid=(D // LANES,), in_specs=[spec, half_spec, half_spec, spec, spec], out_specs=[spec] * 4,
        out_shape=[jax.ShapeDtypeStruct((CW, D), F32)] * 4, scratch_shapes=[pltpu.VMEM((D, LANES), F32)],
        compiler_params=_cparams(),
    )(w, g_mine, g_theirs, m, v)


NORM_NAMES = ("pre_mix_norm", "post_mix_norm", "pre_mlp_norm", "post_mlp_norm")
SMALL_NAMES = NORM_NAMES + ("gdn_conv_w", "fox_f_bias", "gdn_dt_bias", "gdn_a_log", "fox_out_norm", "gdn_out_norm")
CONV_COLS = 3 * DGDN // NCHIP


def _small_gather(d_norms, d_conv, sums, d_fox_norm, d_gdn_norm, loss_row):
    n_arrays = 6
    n_remote = n_arrays * (NDEV - 1)

    def copies_of(src, outs, send_sems, recv_sems):
        x, y, c = _place()
        me = 4 * x + 2 * y + c

        def from_me(chip_index):
            cols = pl.ds(pl.multiple_of(chip_index * CONV_COLS, LANES), CONV_COLS)
            return [src[0], src[1].at[:, cols], src[2], src[3], src[4], src[5]]

        local = [pltpu.make_async_copy(s, outs[a].at[me], send_sems.at[n_remote + a]) for a, s in enumerate(from_me(2 * x + y))]
        remote = []
        for k in range(1, NDEV):
            px, py, pc = x ^ ((k >> 2) & 1), y ^ ((k >> 1) & 1), c ^ (k & 1)
            remote += [pltpu.make_async_remote_copy(
                src_ref=s, dst_ref=outs[a].at[me], send_sem=send_sems.at[n_arrays * (k - 1) + a],
                recv_sem=recv_sems.at[n_arrays * (k - 1) + a], device_id=(px, py, pc), device_id_type=MESH)
                for a, s in enumerate(from_me(2 * px + py))]
        return local + remote

    def start(*refs):
        for cp in copies_of(*refs):
            cp.start()

    def finish(*refs):
        for cp in copies_of(*refs):
            cp.wait()

    shapes = [(4, D), (CONV_K, CONV_COLS), (8, LANES), (1, LANES), (1, LANES), (1, LANES)]
    return _Exchange([d_norms, d_conv, sums, d_fox_norm, d_gdn_norm, loss_row],
                     [jax.ShapeDtypeStruct((NDEV,) + s, F32) for s in shapes], n_remote + n_arrays, start, finish)


def _small_adamw(gathered, ws, ms, vs):
    n = len(SMALL_NAMES)
    ng = len(gathered)

    def body(*refs):
        def total(buf):
            acc = buf[0]
            for i in range(1, NDEV):
                acc = acc + buf[i]
            return acc

        t_norms, t_conv, t_sums, t_fn, t_gn, t_loss = [total(r) for r in refs[:ng]]
        w_refs, m_refs, v_refs = refs[ng:ng + n], refs[ng + n:ng + 2 * n], refs[ng + 2 * n:ng + 3 * n]
        outs = refs[ng + 3 * n:]
        outs[4 * n][...] = t_loss
        grads = [t_norms[i:i + 1, :] for i in range(4)] + [
            t_conv, t_sums[0:1, 0:NFH], t_sums[1:2, 0:NGH], t_sums[2:3, 0:NGH], t_fn[:, 0:FHD], t_gn]
        for a in range(n):
            d, nm, nv = _adamw_math(w_refs[a][...], grads[a], m_refs[a][...], v_refs[a][...])
            outs[a][...] = grads[a]
            outs[n + a][...] = d
            outs[2 * n + a][...] = nm
            outs[3 * n + a][...] = nv

    def whole(arr):
        return pl.BlockSpec(arr.shape, lambda i: (0,) * arr.ndim)

    res = pl.pallas_call(
        body, name="small_adamw", grid=(1,), in_specs=[whole(t) for t in gathered] + [whole(w) for w in ws] * 3,
        out_specs=[whole(w) for w in ws] * 4 + [pl.BlockSpec((1, LANES), lambda i: (0, 0))],
        out_shape=[jax.ShapeDtypeStruct(w.shape, F32) for w in ws] * 4 + [jax.ShapeDtypeStruct((1, LANES), F32)],
        compiler_params=_cparams(),
    )(*gathered, *ws, *ms, *vs)
    return res[:n], res[n:2 * n], res[2 * n:3 * n], res[3 * n:4 * n], res[4 * n]


CW = DPROJ // NCHIP
PROJ_RUNS = tuple((part * DFOX + hp * LANES, part * DFOX + (hp + 1) * LANES, (3 * hp + part) * LANES)
                  for hp in range(NPAIR) for part in range(3)) + (
    (1536, 1544, BLK_SMALL * LANES), (1544, 3080, BLK_GDN * LANES), (3080, 3088, BLK_SMALL * LANES + 8),
    (3088, 3600, BLK_GZ * LANES))


def _proj_pieces():
    pieces = []
    for lo, hi, at in PROJ_RUNS:
        while lo < hi:
            j = lo // CW
            end = min(hi, (j + 1) * CW)
            pieces.append((j, lo - j * CW, at, end - lo))
            at, lo = at + end - lo, end
    return pieces


RT = 256


def _to_padded_rows(gathered):
    def body(src_ref, out_ref, blocks_ref, rows_ref):
        blocks_ref[...] = src_ref[...].astype(F32)
        rows_ref[...] = jnp.zeros_like(rows_ref)
        for j, start, at, n in _proj_pieces():
            rows_ref[at:at + n, :] = blocks_ref[j, start:start + n, :]
        out_ref[...] = rows_ref[...].astype(out_ref.dtype)

    return pl.pallas_call(
        body, name="proj_rows_in", grid=(D // RT,), in_specs=[pl.BlockSpec((NCHIP, D, RT), lambda i: (0, 0, i))],
        out_specs=pl.BlockSpec((DPROJ_PAD, RT), lambda i: (0, i)), out_shape=jax.ShapeDtypeStruct((DPROJ_PAD, D), gathered.dtype),
        scratch_shapes=[pltpu.VMEM((NCHIP, D, RT), F32), pltpu.VMEM((DPROJ_PAD, RT), F32)], compiler_params=_cparams(),
    )(gathered)


def _from_padded_rows_pair_sum(w):
    steps = D // RT
    half = D // 2

    def body(src_ref, out_ref, rows_ref, blocks_ref, mine_ref, send_ref, recv_ref, send_sems, recv_sems):
        i = pl.program_id(0)
        x, y, c = _place()

        def share(t):
            return pltpu.make_async_remote_copy(
                src_ref=send_ref.at[t], dst_ref=recv_ref.at[t], send_sem=send_sems.at[t], recv_sem=recv_sems.at[t],
                device_id=(x, y, 1 - c), device_id_type=MESH)

        def rows_of(core):
            return blocks_ref[:, pl.ds(pl.multiple_of(core * half, half), half), :]

        @pl.when(i < steps)
        def _():
            rows_ref[...] = src_ref[...].astype(F32)
            blocks_ref[...] = jnp.zeros_like(blocks_ref)
            for j, start, at, n in _proj_pieces():
                blocks_ref[j, start:start + n, :] = rows_ref[at:at + n, :]
            mine_ref[i % 2] = rows_of(c)
            send_ref[i] = rows_of(1 - c).astype(BF16)
            share(i).start()

        @pl.when(i > 0)
        def _():
            share(i - 1).wait_recv()
            out_ref[...] = (mine_ref[(i - 1) % 2] + recv_ref[i - 1].astype(F32)).astype(BF16)

        @pl.when(i == steps)
        def _():
            for t in range(steps):
                share(t).wait_send()

    tile = (NCHIP, half, RT)
    return pl.pallas_call(
        body, name="proj_rows_out_pair_sum", grid=(steps + 1,),
        in_specs=[pl.BlockSpec((DPROJ_PAD, RT), lambda i: (0, jnp.minimum(i, steps - 1)))],
        out_specs=pl.BlockSpec(tile, lambda i: (0, 0, jnp.maximum(i - 1, 0))),
        out_shape=jax.ShapeDtypeStruct((NCHIP, half, D), BF16),
        scratch_shapes=[pltpu.VMEM((DPROJ_PAD, RT), F32), pltpu.VMEM((NCHIP, D, RT), F32), pltpu.VMEM((2,) + tile, F32),
                        pltpu.VMEM((steps,) + tile, BF16), pltpu.VMEM((steps,) + tile, BF16),
                        pltpu.SemaphoreType.DMA((steps,)), pltpu.SemaphoreType.DMA((steps,))],
        compiler_params=_cparams(),
    )(w)


def _local_step(x, target, first_weights, late_weights, reduce_late, reduce_in, pre_mix_norm, fox_f_bias, fox_out_norm,
                gdn_a_log, gdn_dt_bias, gdn_out_norm, post_mix_norm, pre_mlp_norm, post_mlp_norm):
    bias_vec = jnp.zeros((1, LANES), F32).at[0, 0:NFH].set(fox_f_bias).at[0, LANE_G:LANE_G + NGH].set(gdn_dt_bias)
    alog_vec = jnp.zeros((1, LANES), F32).at[0, LANE_G:LANE_G + NGH].set(gdn_a_log)
    w2 = jnp.concatenate([fox_out_norm, fox_out_norm], axis=1)

    h, first = _pre_norm(x, pre_mix_norm, exchange=first_weights[0])
    win_p, conv_w = first_weights[1](first)
    proj = _matmul(h, win_p, tb=True, tm=2048, tn=768, tk=1024, name="mm_proj", exchange=late_weights[0])
    proj, late_a = proj if late_weights[0] is not None else (proj, [])
    gates = _gates(proj, bias_vec, alog_vec)
    mix, fox_o, lse, late_b = _fox_fwd(proj, gates, w2, exchange=late_weights[1])
    qkv = _gdn_pre(proj, conv_w)
    (u, w, qd, kd, a_intra, gl, t_inv), _ = _gdn_prep(qkv, gates)
    wout, wup3 = late_weights[3](late_a, late_b)
    mix, gdn_raw, states = _gdn_scan(u, w, qd, kd, a_intra, gl, proj, gdn_out_norm, mix)
    def post_mix(acc, xv, w_post, w_pre_mlp):
        x1v = xv + acc * _rms_scale(acc) * w_post
        return acc, x1v, x1v * _rms_scale(x1v) * w_pre_mlp

    mixed, x1, h2 = _matmul(mix, wout, tm=512, tn=D, tk=1024, out_dtypes=(F32, F32, BF16), name="mm_out",
                            extra=(x, post_mix_norm, pre_mlp_norm), epilogue=post_mix)

    def relu2(acc):
        r = jnp.maximum(acc, 0.0)
        return r, r * r

    up_act = _matmul(h2, wup3, b3=True, tm=1024, tn=1024, tk=1024, out_dtypes=(BF16, BF16), epilogue=relu2,
                     name="mm_up", exchange=late_weights[2])
    (up_relu, act), late_c = up_act if late_weights[2] is not None else (up_act, [])
    wdown = late_weights[4](late_c)
    def loss_head(acc, x1v, tv, w):
        err = x1v + acc * _rms_scale(acc) * w - tv
        dx2v = err * (1.0 / D)
        dyv, dwt = _rms_bwd(acc, w, dx2v)
        part = 0.5 * jnp.sum(jnp.mean(err * err, axis=-1, keepdims=True), axis=0, keepdims=True)
        return dx2v, dyv, jnp.sum(dwt, axis=0, keepdims=True), jnp.broadcast_to(part, (1, D))

    dx2, dy, d_post_mlp, loss_wide = _matmul(
        act, wdown, tm=512, tn=D, tk=DFF, out_dtypes=(F32, BF16, F32, F32), extra=(x1, target, post_mlp_norm),
        epilogue=loss_head, n_sums=2, name="mm_down")
    loss_row = loss_wide[:, :LANES]

    dwdown = _matmul(act, dy, ta=True, tm=1024, tn=1024, tk=2048, out_dtypes=(BF16,), name="mm_dwdown")

    def relu2_bwd(acc, r):
        return (acc * 2.0 * r.astype(F32),)

    dup = _matmul(dy, wdown, tb=True, tm=1024, tn=1024, tk=1024, out_dtypes=(BF16,), extra=(up_relu,), epilogue=relu2_bwd,
                  name="mm_dact")
    dwup3 = _matmul(h2, dup, ta=True, tm=1024, tn=1024, tk=2048, out_dtypes=(BF16,), o3=True, name="mm_dwup")
    def mid_bwd(acc, x1v, dx2v, mixedv, w_pre_mlp, w_post):
        dxa, dwm = _rms_bwd(x1v, w_pre_mlp, acc)
        dx1v = dx2v + dxa
        dm, dwp = _rms_bwd(mixedv, w_post, dx1v)
        return dx1v, dm, jnp.sum(dwm, axis=0, keepdims=True), jnp.sum(dwp, axis=0, keepdims=True)

    dx1, dmixed, d_pre_mlp, d_post_mix = _matmul(
        dup, wup3, tb=True, b3=True, tm=512, tn=D, tk=DFF, out_dtypes=(F32, BF16, F32, F32),
        extra=(x1, dx2, mixed, pre_mlp_norm, post_mix_norm), epilogue=mid_bwd, n_sums=2, name="mm_dh2")
    dwout = _matmul(mix, dmixed, ta=True, tm=1024, tn=1024, tk=2048, out_dtypes=(BF16,), name="mm_dwout")
    dmix = _matmul(dmixed, wout, tb=True, tm=2048, tk=1024, name="mm_dmix")

    dfox, delta, d_fox_norm, from_sibling = _fox_norm_bwd(fox_o, dmix, w2, exchange=reduce_late[0](dwout, dwup3, dwdown))
    dproj, dcum_fox, reduced_a = _fox_bwd(proj, dfox, gates, lse, delta, exchange=reduce_late[1](from_sibling))
    (dproj, du, dw, dqd, dkd, da, dgl, d_gdn_norm), reduced_b = _gdn_scan_bwd(
        dmix, gdn_raw, proj, gdn_out_norm, u, w, qd, kd, a_intra, gl, states, dproj, exchange=reduce_late[2]())
    dqkv, dgates_gdn, reduced_c = _gdn_prep_bwd(qkv, gates, t_inv, du, dw, dqd, dkd, da, dgl, exchange=reduce_late[3]())
    reduced_late = (reduced_a, reduced_b, reduced_c)
    dproj, d_conv = _gdn_pre_bwd(proj, conv_w, dqkv, dproj)
    dproj, sums = _gates_bwd(proj, bias_vec, alog_vec, dgates_gdn, dcum_fox, dproj)

    dwin_p = _matmul(dproj, h, ta=True, tm=1280, tn=1024, tk=2048, out_dtypes=(BF16,), name="mm_dwin")
    exchange_in = reduce_in(dwin_p)
    dh = _matmul(dproj, win_p, tm=1024, tk=DPROJ_PAD, name="mm_dh", exchange=exchange_in)
    dh, reduced_in = dh if exchange_in is not None else (dh, [])
    grad_x, d_pre_mix = _pre_norm_bwd(dh, x, pre_mix_norm, dx1)

    d_norms = jnp.concatenate([d_pre_mix, d_post_mix, d_pre_mlp, d_post_mlp], axis=0)
    return grad_x, (d_norms, d_conv, sums, d_fox_norm, d_gdn_norm, loss_row), reduced_late, reduced_in


def kernel(x, pre_mix_norm, w_in, fox_f_bias, fox_out_norm, gdn_conv_w, gdn_a_log, gdn_dt_bias, gdn_out_norm, w_out, post_mix_norm, pre_mlp_norm, w_up, w_down, post_mlp_norm, loss_target, m_pre_mix_norm, m_w_in, m_fox_f_bias, m_fox_out_norm, m_gdn_conv_w, m_gdn_a_log, m_gdn_dt_bias, m_gdn_out_norm, m_w_out, m_post_mix_norm, m_pre_mlp_norm, m_w_up, m_w_down, m_post_mlp_norm, v_pre_mix_norm, v_w_in, v_fox_f_bias, v_fox_out_norm, v_gdn_conv_w, v_gdn_a_log, v_gdn_dt_bias, v_gdn_out_norm, v_w_out, v_post_mix_norm, v_pre_mlp_norm, v_w_up, v_w_down, v_post_mlp_norm):
    weights = dict(pre_mix_norm=pre_mix_norm, w_in=w_in, fox_f_bias=fox_f_bias, fox_out_norm=fox_out_norm, gdn_conv_w=gdn_conv_w,
                   gdn_a_log=gdn_a_log, gdn_dt_bias=gdn_dt_bias, gdn_out_norm=gdn_out_norm, w_out=w_out, post_mix_norm=post_mix_norm,
                   pre_mlp_norm=pre_mlp_norm, w_up=w_up, w_down=w_down, post_mlp_norm=post_mlp_norm)
    m_in = dict(pre_mix_norm=m_pre_mix_norm, w_in=m_w_in, fox_f_bias=m_fox_f_bias, fox_out_norm=m_fox_out_norm, gdn_conv_w=m_gdn_conv_w,
                gdn_a_log=m_gdn_a_log, gdn_dt_bias=m_gdn_dt_bias, gdn_out_norm=m_gdn_out_norm, w_out=m_w_out, post_mix_norm=m_post_mix_norm,
                pre_mlp_norm=m_pre_mlp_norm, w_up=m_w_up, w_down=m_w_down, post_mlp_norm=m_post_mlp_norm)
    v_in = dict(pre_mix_norm=v_pre_mix_norm, w_in=v_w_in, fox_f_bias=v_fox_f_bias, fox_out_norm=v_fox_out_norm, gdn_conv_w=v_gdn_conv_w,
                gdn_a_log=v_gdn_a_log, gdn_dt_bias=v_gdn_dt_bias, gdn_out_norm=v_gdn_out_norm, w_out=v_w_out, post_mix_norm=v_post_mix_norm,
                pre_mlp_norm=v_pre_mlp_norm, w_up=v_w_up, w_down=v_w_down, post_mlp_norm=v_post_mlp_norm)
    order_w = ("pre_mix_norm", "w_in", "fox_f_bias", "fox_out_norm", "gdn_conv_w", "gdn_a_log", "gdn_dt_bias", "gdn_out_norm", "w_out",
               "post_mix_norm", "pre_mlp_norm", "w_up", "w_down", "post_mlp_norm")
    big = ("w_in", "w_out", "w_up", "w_down")

    def row(v):
        return v if v.ndim == 2 else v.reshape(1, -1)

    win_shard = jnp.pad(w_in.T.astype(BF16), ((0, D - CW), (0, 0)))

    def resolve_first(gathered):
        win_g, conv_g = gathered
        return (_to_padded_rows(_with_own(win_g, win_shard)),
                _with_own(conv_g, gdn_conv_w).transpose(1, 0, 2).reshape(CONV_K, 3 * DGDN))

    late_shards = [weights[n].astype(BF16) for n in big[1:]]

    gathered_down = []

    def resolve_out_up(gathered_out, gathered_mlp):
        gathered_down.append(gathered_mlp[1])
        return _with_own(gathered_out[0], late_shards[0]).reshape(D, D), _with_own(gathered_mlp[0], late_shards[1])

    def resolve_down(_):
        return _with_own(gathered_down[0], late_shards[2]).reshape(DFF, D)

    pair_sums, late_blocks = {}, []

    def pair_summed(names, blocks, theirs):
        for n, s in zip(names, _pair_sum(blocks, theirs, "grads_pair_sum_" + names[0])):
            pair_sums[n] = s

    def late_pair_exchange(dwout, dwup3, dwdown):
        late_blocks.extend([dwout.reshape(NCHIP, D // NCHIP, D), dwup3, dwdown.reshape(NCHIP, DFF // NCHIP, D)])
        return _pair_exchange(late_blocks)

    def late_chip_exchange(theirs):
        pair_summed(big[1:], late_blocks, theirs)
        return _chip_exchange([pair_sums[n] for n in big[1:]])

    def reduce_in(dwin_p):
        pair_sums["w_in"] = _from_padded_rows_pair_sum(dwin_p)
        return _chip_exchange([pair_sums["w_in"]])

    grad_x, small, received_late, received_in = _local_step(
        x[0], loss_target[0], (_allgather_exchange([win_shard], whole=[gdn_conv_w]), resolve_first),
        (_allgather_exchange(late_shards[:1]), _allgather_exchange(late_shards[1:]), None, resolve_out_up, resolve_down),
        (late_pair_exchange, late_chip_exchange, lambda: None, lambda: None),
        reduce_in, row(pre_mix_norm), fox_f_bias, row(fox_out_norm), gdn_a_log, gdn_dt_bias,
        row(gdn_out_norm), row(post_mix_norm), row(pre_mlp_norm), row(post_mlp_norm))

    g_mine, g_theirs, small_gathered = _chip_sum(
        [pair_sums[n] for n in big], list(received_in[:1]) + list(received_late[0][:3]), exchange=_small_gather(*small))

    g_big, d_big, nm_big, nv_big, _ = _adamw_big(
        [weights[n] for n in big[1:]], g_mine[1:], g_theirs[1:], [m_in[n] for n in big[1:]], [v_in[n] for n in big[1:]])
    in_t = _adamw_in(w_in.T, g_mine[0], g_theirs[0], m_w_in.T, v_w_in.T)
    g_small, d_small, nm_small, nv_small, loss_total = _small_adamw(
        small_gathered, [row(weights[n]) for n in SMALL_NAMES], [row(m_in[n]) for n in SMALL_NAMES],
        [row(v_in[n]) for n in SMALL_NAMES])

    grads, delta, new_m, new_v = {}, {}, {}, {}
    grads["w_in"], delta["w_in"], new_m["w_in"], new_v["w_in"] = [t.T for t in in_t]
    for i, n in enumerate(big[1:]):
        grads[n], delta[n], new_m[n], new_v[n] = g_big[i], d_big[i], nm_big[i], nv_big[i]
    for i, n in enumerate(SMALL_NAMES):
        shape = weights[n].shape
        grads[n], delta[n], new_m[n], new_v[n] = (g_small[i].reshape(shape), d_small[i].reshape(shape),
                                                  nm_small[i].reshape(shape), nv_small[i].reshape(shape))
    return (loss_total[0, 0], grad_x[None], *[grads[n] for n in order_w], *[delta[n] for n in order_w], *[new_m[n] for n in order_w],
            *[new_v[n] for n in order_w])
```

```python
import jax
import jax.numpy as jnp
from jax import lax
from jax.experimental import pallas as pl
from jax.experimental.pallas import tpu as pltpu

F32 = jnp.float32
BF16 = jnp.bfloat16
MESH = pl.DeviceIdType.MESH

S = 2048
D = 1024
NFH, FHD = 8, 64
NPAIR = NFH // 2
NGH, GHD = 4, 128
DFOX = NFH * FHD
DGDN = NGH * GHD
CHUNK = 64
NCH = S // CHUNK
CONV_K = 4
DFF = 4 * D
EPS = 1e-6
DPROJ = 3600
LANES = 128
DPROJ_PAD = 3840
BLK_GDN = 12
BLK_GZ = 24
BLK_SMALL = 28
NCHIP = 4
NDEV = 8
VMEM_LIMIT = 56 * 1024 * 1024

ADAM_LR = 0.001
ADAM_B1 = 0.9
ADAM_B2 = 0.999
ADAM_EPS = 1e-08
ADAM_WD = 0.01
ADAM_STEP = 10


def _cparams(**kw):
    return pltpu.CompilerParams(vmem_limit_bytes=VMEM_LIMIT, **kw)


def _dn(ca, cb):
    return (((ca,), (cb,)), ((), ()))


def _dot(a, b, ca=1, cb=0):
    return lax.dot_general(a.astype(BF16), b.astype(BF16), _dn(ca, cb), preferred_element_type=F32)


def _hdot(a, b, ca=1, cb=0):
    return lax.dot_general(a.astype(F32), b.astype(F32), _dn(ca, cb), precision=lax.Precision.HIGHEST,
                           preferred_element_type=F32)


def _dot3(a, b, ca=1, cb=0):
    a_hi, b_hi = a.astype(BF16), b.astype(BF16)
    a_lo, b_lo = (a - a_hi.astype(F32)).astype(BF16), (b - b_hi.astype(F32)).astype(BF16)
    dn = _dn(ca, cb)
    return (lax.dot_general(a_hi, b_hi, dn, preferred_element_type=F32)
            + (lax.dot_general(a_hi, b_lo, dn, preferred_element_type=F32)
               + lax.dot_general(a_lo, b_hi, dn, preferred_element_type=F32)))


@jax.custom_vjp
def _mm_nn(a, b):
    return _dot(a, b, 1, 0)


def _mm_nn_fwd(a, b):
    return _dot(a, b, 1, 0), (a, b)


def _mm_nn_bwd(res, g):
    a, b = res
    return _dot(g, b, 1, 1), _dot(a, g, 0, 0)


_mm_nn.defvjp(_mm_nn_fwd, _mm_nn_bwd)


@jax.custom_vjp
def _mm_nt(a, b):
    return _dot(a, b, 1, 1)


def _mm_nt_fwd(a, b):
    return _dot(a, b, 1, 1), (a, b)


def _mm_nt_bwd(res, g):
    a, b = res
    return _dot(g, b, 1, 0), _dot(g, a, 0, 0)


_mm_nt.defvjp(_mm_nt_fwd, _mm_nt_bwd)


@jax.custom_vjp
def _saved_inverse(m, t_inv):
    del m
    return t_inv


def _saved_inverse_fwd(m, t_inv):
    del m
    return t_inv, t_inv


def _saved_inverse_bwd(t_inv, g):
    return -_dot3(_dot3(t_inv, g, 0, 0), t_inv, 1, 1), jnp.zeros_like(t_inv)


_saved_inverse.defvjp(_saved_inverse_fwd, _saved_inverse_bwd)


def _sigmoid(z):
    return 1.0 / (1.0 + jnp.exp(-z))


def _softplus(z):
    return jnp.maximum(z, 0.0) + jnp.log(1.0 + jnp.exp(-jnp.abs(z)))


def _silu(z):
    return z * _sigmoid(z)


def _rms_scale(x):
    return lax.rsqrt(jnp.mean(x * x, axis=-1, keepdims=True) + EPS)


def _rms_bwd(x, w, g):
    r = _rms_scale(x)
    gw = g * w
    dx = r * gw - x * (r * r * r) * jnp.mean(gw * x, axis=-1, keepdims=True)
    return dx, g * x * r


def _matmul(a, b, *, name, ta=False, tb=False, tm=512, tn=512, tk=512, out_dtypes=(F32,), b3=False, o3=False,
            extra=(), epilogue=None, exchange=None, n_sums=0):
    m, k = (a.shape[1], a.shape[0]) if ta else a.shape
    if b3:
        n = b.shape[1] if tb else b.shape[0] * b.shape[2]
        kb = b.shape[0] * b.shape[2] if tb else b.shape[1]
    else:
        n, kb = (b.shape[0], b.shape[1]) if tb else (b.shape[1], b.shape[0])
    assert kb == k, (name, kb, k)
    tm, tn, tk = min(tm, m), min(tn, n), min(tk, k)
    assert m % tm == 0 and n % tn == 0 and k % tk == 0, (name, m, n, k, tm, tn, tk)
    nk = k // tk
    whole_k_blocks = b3 and tb and not ta and nk == 1 and b.shape[0] > 1
    n_extra = len(extra)
    n_out = len(out_dtypes)
    grid = (m // tm, n // tn, nk)
    ex_in, ex_in_specs, ex_out_specs, ex_out_shape, ex_scratch = _hosted(exchange)

    def body(*refs):
        a_ref, b_ref = refs[0], refs[1]
        extra_refs = refs[2:2 + n_extra]
        first_out = 2 + n_extra + len(ex_in)
        out_refs = refs[first_out:first_out + n_out]
        ex_refs = refs[2 + n_extra:first_out] + refs[first_out + n_out:first_out + n_out + len(ex_out_shape)] + refs[-2:]
        step = [pl.program_id(d) for d in range(3)]

        if exchange is not None:
            @pl.when((step[0] == 0) & (step[1] == 0) & (step[2] == 0))
            def _():
                exchange.start(*exchange.split(ex_refs))

        def finish(acc):
            outs = (acc,) if epilogue is None else epilogue(acc, *[r[...] for r in extra_refs])
            for o_ref, val in zip(out_refs[:n_out - n_sums], outs):
                o_ref[...] = val.astype(o_ref.dtype)
            for o_ref, val in zip(out_refs[n_out - n_sums:], outs[n_out - n_sums:]):
                @pl.when(step[0] == 0)
                def _(o_ref=o_ref, val=val):
                    o_ref[...] = val

                @pl.when(step[0] > 0)
                def _(o_ref=o_ref, val=val):
                    o_ref[...] += val

        if whole_k_blocks:
            width = b.shape[2]
            part = _dot(a_ref[:, 0:width], b_ref[0], 1, 1)
            for blk in range(1, b.shape[0]):
                part = part + _dot(a_ref[:, blk * width:(blk + 1) * width], b_ref[blk], 1, 1)
        else:
            part = _dot(a_ref[...], b_ref[...], 0 if ta else 1, 1 if tb else 0)
        if nk == 1:
            finish(part)
        else:
            acc_ref = refs[first_out + n_out + len(ex_out_shape)]

            @pl.when(step[2] == 0)
            def _():
                acc_ref[...] = part

            @pl.when(step[2] > 0)
            def _():
                acc_ref[...] += part

            @pl.when(step[2] == nk - 1)
            def _():
                finish(acc_ref[...])

        if exchange is not None:
            flat = (step[0] * grid[1] + step[1]) * nk + step[2]
            total = grid[0] * grid[1] * nk

            @pl.when(flat == total // 2)
            def _():
                exchange.middle(*exchange.split(ex_refs))

            @pl.when(flat == total - 1)
            def _():
                exchange.rest(*exchange.split(ex_refs))

    a_spec = pl.BlockSpec((tk, tm), lambda i, j, kk: (kk, i)) if ta else pl.BlockSpec((tm, tk), lambda i, j, kk: (i, kk))
    if whole_k_blocks:
        b_spec = pl.BlockSpec((b.shape[0], tn, b.shape[2]), lambda i, j, kk: (0, j, 0))
    elif b3 and tb:
        assert b.shape[2] == tk
        b_spec = pl.BlockSpec((None, tn, tk), lambda i, j, kk: (kk, j, 0))
    elif b3:
        assert b.shape[2] == tn
        b_spec = pl.BlockSpec((None, tk, tn), lambda i, j, kk: (j, kk, 0))
    elif tb:
        b_spec = pl.BlockSpec((tn, tk), lambda i, j, kk: (j, kk))
    else:
        b_spec = pl.BlockSpec((tk, tn), lambda i, j, kk: (kk, j))
    tile = pl.BlockSpec((tm, tn), lambda i, j, kk: (i, j))
    out_specs = [tile] * n_out
    out_shape = [jax.ShapeDtypeStruct((m, n), dt) for dt in out_dtypes]
    if o3:
        out_specs[0] = pl.BlockSpec((None, tm, tn), lambda i, j, kk: (j, i, 0))
        out_shape[0] = jax.ShapeDtypeStruct((n // tn, m, tn), out_dtypes[0])
    assert n_sums == 0 or tn == n
    for r in range(n_out - n_sums, n_out):
        out_specs[r] = pl.BlockSpec((1, tn), lambda i, j, kk: (0, 0))
        out_shape[r] = jax.ShapeDtypeStruct((1, n), out_dtypes[r])
    res = pl.pallas_call(
        body, name=name, grid=grid,
        in_specs=[a_spec, b_spec] + [tile if e.shape[0] == m else pl.BlockSpec((1, tn), lambda i, j, kk: (0, j)) for e in extra]
        + ex_in_specs, out_specs=out_specs + ex_out_specs,
        out_shape=out_shape + ex_out_shape,
        scratch_shapes=([pltpu.VMEM((tm, tn), F32)] if nk > 1 else []) + ex_scratch,
        compiler_params=_cparams(),
    )(a, b, *extra, *ex_in)
    if exchange is not None:
        return (res[0] if n_out == 1 else res[:n_out]), res[n_out:]
    return res[0] if n_out == 1 else res


TR = 256


def _row_spec(cols):
    return pl.BlockSpec((TR, cols), lambda i: (i, 0))


def _vec_spec(cols):
    return pl.BlockSpec((1, cols), lambda i: (0, 0))


def _pre_norm(x, w, exchange=None):
    ex_in, ex_in_specs, ex_out_specs, ex_out_shape, ex_scratch = _hosted(exchange)

    def body(*refs):
        x_ref, w_ref, h_ref = refs[0], refs[1], refs[2 + len(ex_in)]
        ex_refs = refs[2:2 + len(ex_in)] + refs[3 + len(ex_in):]
        if exchange is not None:
            @pl.when(pl.program_id(0) == 0)
            def _():
                exchange.start(*exchange.split(ex_refs))

        xv = x_ref[...]
        h_ref[...] = (xv * _rms_scale(xv) * w_ref[...]).astype(BF16)

        if exchange is not None:
            @pl.when(pl.program_id(0) == S // TR - 1)
            def _():
                exchange.finish(*exchange.split(ex_refs))

    res = pl.pallas_call(
        body, name="pre_norm", grid=(S // TR,), in_specs=[_row_spec(D), _vec_spec(D)] + ex_in_specs,
        out_specs=[_row_spec(D)] + ex_out_specs, out_shape=[jax.ShapeDtypeStruct((S, D), BF16)] + ex_out_shape,
        scratch_shapes=ex_scratch, compiler_params=_cparams(),
    )(x, w, *ex_in)
    return res[0], res[1:]


def _pre_norm_bwd(dh, x, w, dx1):
    def body(dh_ref, x_ref, w_ref, dx1_ref, dx_ref, dw_ref):
        i = pl.program_id(0)
        dxa, dwt = _rms_bwd(x_ref[...], w_ref[...], dh_ref[...])
        dx_ref[...] = dx1_ref[...] + dxa

        @pl.when(i == 0)
        def _():
            dw_ref[...] = jnp.zeros_like(dw_ref)

        dw_ref[...] += jnp.sum(dwt, axis=0, keepdims=True)

    return pl.pallas_call(
        body, name="pre_norm_bwd", grid=(S // TR,),
        in_specs=[_row_spec(D), _row_spec(D), _vec_spec(D), _row_spec(D)], out_specs=[_row_spec(D), _vec_spec(D)],
        out_shape=[jax.ShapeDtypeStruct((S, D), F32), jax.ShapeDtypeStruct((1, D), F32)], compiler_params=_cparams(),
    )(dh, x, w, dx1)


BQ = 512
NQ = S // BQ
LANE_BETA, LANE_G = 8, 12


def _gate_lanes(shape):
    lane = lax.broadcasted_iota(jnp.int32, shape, 1)
    return lane < LANE_BETA, (lane >= LANE_BETA) & (lane < LANE_G), (lane >= LANE_G) & (lane < LANE_G + NGH)


def _gates(proj, bias_vec, alog_vec):
    def body(s_ref, b_ref, a_ref, o_ref, carry_ref):
        i = pl.program_id(0)

        @pl.when(i == 0)
        def _():
            carry_ref[...] = jnp.zeros_like(carry_ref)

        z = s_ref[...] + b_ref[...]
        tail = jnp.log(1.0 + jnp.exp(-jnp.abs(z)))
        sp = jnp.maximum(z, 0.0) + tail
        lf = jnp.minimum(z, 0.0) - tail
        r = lax.broadcasted_iota(jnp.int32, (BQ, BQ), 0)
        c = lax.broadcasted_iota(jnp.int32, (BQ, BQ), 1)
        tri = (c <= r).astype(F32)
        cum = _hdot(tri, lf) + carry_ref[...]
        carry_ref[...] = cum[BQ - 1:BQ, :]
        is_fox, is_beta, is_g = _gate_lanes(z.shape)
        o_ref[...] = jnp.where(is_fox, cum, jnp.where(is_beta, _sigmoid(z), jnp.where(is_g, -jnp.exp(a_ref[...]) * sp, 0.0)))

    return pl.pallas_call(
        body, name="gates", grid=(NQ,),
        in_specs=[pl.BlockSpec((BQ, LANES), lambda i: (i, BLK_SMALL)), _vec_spec(LANES), _vec_spec(LANES)],
        out_specs=pl.BlockSpec((BQ, LANES), lambda i: (i, 0)), out_shape=jax.ShapeDtypeStruct((S, LANES), F32),
        scratch_shapes=[pltpu.VMEM((1, LANES), F32)], compiler_params=_cparams(),
    )(proj, bias_vec, alog_vec)


def _gates_bwd(proj, bias_vec, alog_vec, dgates_gdn, dcum_fox, dproj):
    def body(s_ref, b_ref, a_ref, dg_ref, dc_ref, dproj_in, dproj_ref, red_ref, carry_ref):
        del dproj_in
        i = pl.program_id(0)

        @pl.when(i == 0)
        def _():
            carry_ref[...] = jnp.zeros_like(carry_ref)
            red_ref[...] = jnp.zeros_like(red_ref)

        z = s_ref[...] + b_ref[...]
        dg = dg_ref[...] + dc_ref[...]
        r = lax.broadcasted_iota(jnp.int32, (BQ, BQ), 0)
        c = lax.broadcasted_iota(jnp.int32, (BQ, BQ), 1)
        upper = (c >= r).astype(F32)
        dlf = _hdot(upper, dg) + carry_ref[...]
        carry_ref[...] = dlf[0:1, :]
        sig = _sigmoid(z)
        g_scale = -jnp.exp(a_ref[...])
        is_fox, is_beta, is_g = _gate_lanes(z.shape)
        ds = jnp.where(is_fox, dlf * (1.0 - sig), jnp.where(is_beta, dg * sig * (1.0 - sig), jnp.where(is_g, dg * g_scale * sig, 0.0)))
        dproj_ref[:, 0:LANES] = ds.astype(BF16)
        dproj_ref[:, LANES:2 * LANES] = jnp.zeros((BQ, LANES), BF16)
        dalog = jnp.where(is_g, dg * g_scale * _softplus(z), 0.0)
        sums = jnp.sum(ds, axis=0, keepdims=True)
        red_ref[0:1, :] += jnp.where(is_fox[0:1], sums, 0.0)
        red_ref[1:2, :] += pltpu.roll(jnp.where(is_g[0:1], sums, 0.0), LANES - LANE_G, 1)
        red_ref[2:3, :] += pltpu.roll(jnp.sum(dalog, axis=0, keepdims=True), LANES - LANE_G, 1)

    blk = pl.BlockSpec((BQ, LANES), lambda i: (NQ - 1 - i, 0))
    return pl.pallas_call(
        body, name="gates_bwd", grid=(NQ,),
        in_specs=[pl.BlockSpec((BQ, LANES), lambda i: (NQ - 1 - i, BLK_SMALL)), _vec_spec(LANES), _vec_spec(LANES), blk, blk,
                  pl.BlockSpec(memory_space=pl.ANY)],
        out_specs=[pl.BlockSpec((BQ, 2 * LANES), lambda i: (NQ - 1 - i, BLK_SMALL // 2)), pl.BlockSpec((8, LANES), lambda i: (0, 0))],
        out_shape=[jax.ShapeDtypeStruct((S, DPROJ_PAD), BF16), jax.ShapeDtypeStruct((8, LANES), F32)],
        input_output_aliases={5: 0},
        scratch_shapes=[pltpu.VMEM((1, LANES), F32)], compiler_params=_cparams(),
    )(proj, bias_vec, alog_vec, dgates_gdn, dcum_fox, dproj)


FOX_SCALE = FHD ** -0.5
FOX_PAIRS = 2
FOX_PAIRS_BWD = 2


def _head_mask(e):
    lane = lax.broadcasted_iota(jnp.int32, (1, LANES), 1)
    return (lane >= e * FHD) & (lane < (e + 1) * FHD)


def _lane_col(vals, index):
    lane = lax.broadcasted_iota(jnp.int32, vals.shape, 1)
    return jnp.sum(jnp.where(lane == index, vals, 0.0), axis=1, keepdims=True)


def _sublane_row(vals, index):
    row = lax.broadcasted_iota(jnp.int32, vals.shape, 0)
    return jnp.sum(jnp.where(row == index, vals, 0.0), axis=0, keepdims=True)


def _pair_cols(c0, c1):
    lane = lax.broadcasted_iota(jnp.int32, (c0.shape[0], 2), 1)
    return jnp.where(lane == 0, c0, c1)


def _split3(x):
    hi = x.astype(BF16).astype(F32)
    rest = x - hi
    mid = rest.astype(BF16).astype(F32)
    return hi, mid, (rest - mid).astype(BF16).astype(F32)


def _fox_operand(vals, e, cum, is_query):
    lane = lax.broadcasted_iota(jnp.int32, (1, LANES), 1)
    base = (1 - e) * FHD
    parts = _split3(cum)
    own = jnp.where(_head_mask(e), vals * FOX_SCALE if is_query else vals, 0.0)
    cum_at, ones_at = (base, base + 3) if is_query else (base + 3, base)
    sign = 1.0 if is_query else -1.0
    out = own + jnp.where((lane >= ones_at) & (lane < ones_at + 3), 1.0, 0.0)
    for i, part in enumerate(parts):
        out = out + jnp.where(lane == cum_at + i, sign * part, 0.0)
    return out.astype(BF16)


def _causal_block():
    return lax.broadcasted_iota(jnp.int32, (BQ, BQ), 1) <= lax.broadcasted_iota(jnp.int32, (BQ, BQ), 0)


def _head_rms(o, masks):
    o2 = o * o
    r = [lax.rsqrt(jnp.sum(jnp.where(mk, o2, 0.0), axis=1, keepdims=True) * (1.0 / FHD) + EPS) for mk in masks]
    return jnp.where(masks[0], r[0], r[1])


def _hosted(exchange):
    if exchange is None:
        return [], [], [], [], []
    return (exchange.inputs, [HBM] * len(exchange.inputs), [HBM] * len(exchange.out_shape), exchange.out_shape,
            exchange.sem_shapes())


def _fox_fwd(proj, gates, w2, exchange=None):
    ex_in, ex_in_specs, ex_out_specs, ex_out_shape, ex_scratch = _hosted(exchange)

    n_in = 3 * FOX_PAIRS + 2
    heads = [(pp, e) for pp in range(FOX_PAIRS) for e in range(2)]

    def body(*refs):
        qkv_refs, g_ref, w_ref = refs[:3 * FOX_PAIRS], refs[3 * FOX_PAIRS], refs[3 * FOX_PAIRS + 1]
        mix_ref, o_ref, lse_ref = refs[n_in + len(ex_in):n_in + 3 + len(ex_in)]
        ka_ref, vb_ref = refs[n_in + 3 + len(ex_in) + len(ex_out_shape):n_in + 5 + len(ex_in) + len(ex_out_shape)]
        ex_refs = refs[n_in:n_in + len(ex_in)] + refs[n_in + 3 + len(ex_in):n_in + 3 + len(ex_in) + len(ex_out_shape)] + refs[-2:]
        grp, qi = pl.program_id(0), pl.program_id(1)

        def head_index(pp, e):
            return 2 * (FOX_PAIRS * grp + pp) + e

        if exchange is not None:
            @pl.when((grp == 0) & (qi == 0))
            def _():
                exchange.start(*exchange.split(ex_refs))

        @pl.when(qi == 0)
        def _():
            gt = g_ref[...]
            for pp in range(FOX_PAIRS):
                kv = qkv_refs[3 * pp + 1][...]
                for e in range(2):
                    ka_ref[2 * pp + e] = _fox_operand(kv, e, _lane_col(gt, head_index(pp, e)), False)
                vb_ref[pp] = qkv_refs[3 * pp + 2][...].astype(BF16)

        masks = [_head_mask(0), _head_mask(1)]
        gt = g_ref[pl.ds(pl.multiple_of(qi * BQ, BQ), BQ), :]
        qs = [_fox_operand(qkv_refs[3 * pp][...], e, _lane_col(gt, head_index(pp, e)), True) for pp, e in heads]
        n = range(len(heads))

        def block(kj, carry, diagonal):
            rows = pl.ds(pl.multiple_of(kj * BQ, BQ), BQ)
            s = [_dot(qs[i], ka_ref[i, rows, :], 1, 1) for i in n]
            if diagonal:
                s = [jnp.where(_causal_block(), s[i], -jnp.inf) for i in n]
            m_new = [jnp.maximum(carry[i][0], jnp.max(s[i], axis=-1, keepdims=True)) for i in n]
            p = [jnp.exp(s[i] - m_new[i]) for i in n]
            alpha = [jnp.exp(carry[i][0] - m_new[i]) for i in n]
            l_new = [alpha[i] * carry[i][1] + jnp.sum(p[i], axis=-1, keepdims=True) for i in n]
            pv = [_dot(p[i], vb_ref[heads[i][0], rows, :]) for i in n]
            return tuple((m_new[i], l_new[i], alpha[i] * carry[i][2] + pv[i]) for i in n)

        one = (jnp.full((BQ, 1), -jnp.inf, F32), jnp.zeros((BQ, 1), F32), jnp.zeros((BQ, LANES), F32))
        below = lax.fori_loop(0, qi, lambda kj, carry: block(kj, carry, False), (one,) * len(heads))
        done = block(qi, below, True)
        for pp in range(FOX_PAIRS):
            (m0, l0, a0), (m1, l1, a1) = done[2 * pp], done[2 * pp + 1]
            o = jnp.where(masks[0], a0 / l0, a1 / l1)
            cols = slice(pp * LANES, (pp + 1) * LANES)
            o_ref[:, cols] = o
            mix_ref[:, cols] = (o * _head_rms(o, masks) * w_ref[...]).astype(BF16)
            lse_ref[pp] = _pair_cols(m0 + jnp.log(l0), m1 + jnp.log(l1))

        if exchange is not None:
            @pl.when((grp == NPAIR // FOX_PAIRS // 2) & (qi == 0))
            def _():
                exchange.middle(*exchange.split(ex_refs))

            @pl.when((grp == NPAIR // FOX_PAIRS - 1) & (qi == NQ - 1))
            def _():
                exchange.rest(*exchange.split(ex_refs))

    qkv_specs = []
    for pp in range(FOX_PAIRS):
        qkv_specs.append(pl.BlockSpec((BQ, LANES), lambda g, i, pp=pp: (i, 3 * (FOX_PAIRS * g + pp))))
        qkv_specs.append(pl.BlockSpec((S, LANES), lambda g, i, pp=pp: (0, 3 * (FOX_PAIRS * g + pp) + 1)))
        qkv_specs.append(pl.BlockSpec((S, LANES), lambda g, i, pp=pp: (0, 3 * (FOX_PAIRS * g + pp) + 2)))
    blk = pl.BlockSpec((BQ, FOX_PAIRS * LANES), lambda g, i: (i, g))
    res = pl.pallas_call(
        body, name="fox_fwd", grid=(NPAIR // FOX_PAIRS, NQ),
        in_specs=qkv_specs + [pl.BlockSpec((S, LANES), lambda g, i: (0, 0)), pl.BlockSpec((1, LANES), lambda g, i: (0, 0))]
        + ex_in_specs,
        out_specs=[blk, blk, pl.BlockSpec((FOX_PAIRS, BQ, 2), lambda g, i: (g, i, 0))] + ex_out_specs,
        out_shape=[jax.ShapeDtypeStruct((S, D), BF16), jax.ShapeDtypeStruct((S, DFOX), F32),
                   jax.ShapeDtypeStruct((NPAIR, S, 2), F32)] + ex_out_shape,
        scratch_shapes=[pltpu.VMEM((2 * FOX_PAIRS, S, LANES), BF16), pltpu.VMEM((FOX_PAIRS, S, LANES), BF16)] + ex_scratch,
        compiler_params=_cparams(),
    )(*([proj] * (3 * FOX_PAIRS)), gates, w2, *ex_in)
    return res[0], res[1], res[2], res[3:]


def _fox_norm_bwd(o, dmix, w2, exchange=None):
    ex_in, ex_in_specs, ex_out_specs, ex_out_shape, ex_scratch = _hosted(exchange)

    def body(*refs):
        o_ref, g_ref, w_ref = refs[:3]
        do_ref, dl_ref, dw_ref = refs[3 + len(ex_in):6 + len(ex_in)]
        ex_refs = refs[3:3 + len(ex_in)] + refs[6 + len(ex_in):]
        hp, qi = pl.program_id(0), pl.program_id(1)

        if exchange is not None:
            @pl.when((hp == 0) & (qi == 0))
            def _():
                exchange.start(*exchange.split(ex_refs))

        masks = [_head_mask(0), _head_mask(1)]
        ov = o_ref[...]
        g = g_ref[...]
        r = _head_rms(ov, masks)
        gw = g * w_ref[...]
        gwo = gw * ov
        mean = [jnp.sum(jnp.where(mk, gwo, 0.0), axis=1, keepdims=True) * (1.0 / FHD) for mk in masks]
        do = r * gw - ov * (r * r * r) * jnp.where(masks[0], mean[0], mean[1])
        do_ref[...] = do.astype(BF16)
        doo = do * ov
        dl_ref[...] = _pair_cols(*[jnp.sum(jnp.where(mk, doo, 0.0), axis=1, keepdims=True) for mk in masks])

        @pl.when((hp == 0) & (qi == 0))
        def _():
            dw_ref[...] = jnp.zeros_like(dw_ref)

        dw_ref[...] += jnp.sum(g * ov * r, axis=0, keepdims=True)

        @pl.when((hp == NPAIR - 1) & (qi == NQ - 1))
        def _():
            dw = dw_ref[...]
            dw_ref[...] = dw + pltpu.roll(dw, FHD, 1)
            if exchange is not None:
                exchange.finish(*exchange.split(ex_refs))

    blk = pl.BlockSpec((BQ, LANES), lambda hp, i: (i, hp))
    vec = pl.BlockSpec((1, LANES), lambda hp, i: (0, 0))
    res = pl.pallas_call(
        body, name="fox_norm_bwd", grid=(NPAIR, NQ), in_specs=[blk, blk, vec] + ex_in_specs,
        out_specs=[blk, pl.BlockSpec((None, BQ, 2), lambda hp, i: (hp, i, 0)), vec] + ex_out_specs,
        out_shape=[jax.ShapeDtypeStruct((S, DFOX), BF16), jax.ShapeDtypeStruct((NPAIR, S, 2), F32),
                   jax.ShapeDtypeStruct((1, LANES), F32)] + ex_out_shape,
        scratch_shapes=ex_scratch, compiler_params=_cparams(),
    )(o, dmix, w2, *ex_in)
    return res[0], res[1], res[2], res[3:]


def _fox_bwd(proj, do, gates, lse, delta, exchange=None):
    ex_in, ex_in_specs, ex_out_specs, ex_out_shape, ex_scratch = _hosted(exchange)

    pg = FOX_PAIRS_BWD
    n_in = 3 * pg + 4
    heads = [(pp, e) for pp in range(pg) for e in range(2)]

    def body(*refs):
        qkv_refs = refs[:3 * pg]
        do_ref, g_ref, lse_ref, dl_ref = refs[3 * pg:n_in]
        dproj_ref, dc_ref = refs[n_in + len(ex_in):n_in + 2 + len(ex_in)]
        qa_ref, dq_ref = refs[n_in + 2 + len(ex_in) + len(ex_out_shape):n_in + 4 + len(ex_in) + len(ex_out_shape)]
        ex_refs = refs[n_in:n_in + len(ex_in)] + refs[n_in + 2 + len(ex_in):n_in + 2 + len(ex_in) + len(ex_out_shape)] + refs[-2:]
        grp, kj = pl.program_id(0), pl.program_id(1)

        def head_index(pp, e):
            return 2 * (pg * grp + pp) + e

        if exchange is not None:
            @pl.when((grp == 0) & (kj == 0))
            def _():
                exchange.start(*exchange.split(ex_refs))

        @pl.when(kj == 0)
        def _():
            gt = g_ref[...]
            for pp in range(pg):
                qv = qkv_refs[3 * pp][...]
                for e in range(2):
                    qa_ref[2 * pp + e] = _fox_operand(qv, e, _lane_col(gt, head_index(pp, e)), True)
            dq_ref[...] = jnp.zeros_like(dq_ref)

        @pl.when((grp == 0) & (kj == 0))
        def _():
            dc_ref[...] = jnp.zeros_like(dc_ref)

        masks = [_head_mask(0), _head_mask(1)]
        krows = pl.ds(pl.multiple_of(kj * BQ, BQ), BQ)
        gk = g_ref[krows, :]
        kas = [_fox_operand(qkv_refs[3 * pp + 1][...], e, _lane_col(gk, head_index(pp, e)), False) for pp, e in heads]
        vbs = [qkv_refs[3 * pp + 2][...].astype(BF16) for pp in range(pg)]
        lane = lax.broadcasted_iota(jnp.int32, (BQ, LANES), 1)
        n = range(len(heads))

        def block(qi, carry, diagonal):
            dks, dvs, css = carry
            rows = pl.ds(pl.multiple_of(qi * BQ, BQ), BQ)
            qa = [qa_ref[i, rows, :] for i in n]
            s = [_dot(qa[i], kas[i], 1, 1) for i in n]
            if diagonal:
                s = [jnp.where(_causal_block(), s[i], -jnp.inf) for i in n]
            dov = [do_ref[rows, pp * LANES:(pp + 1) * LANES] for pp in range(pg)]
            doe = [jnp.where(masks[e], dov[pp], jnp.zeros_like(dov[pp])) for pp, e in heads]
            lse2 = [lse_ref[pp, rows, :] for pp in range(pg)]
            dl2 = [dl_ref[pp, rows, :] for pp in range(pg)]
            p = [jnp.exp(s[i] - _lane_col(lse2[heads[i][0]], heads[i][1])) for i in n]
            dp = [_dot(doe[i], vbs[heads[i][0]], 1, 1) for i in n]
            ds = [p[i] * (dp[i] - _lane_col(dl2[heads[i][0]], heads[i][1])) for i in n]
            dv_part = [_dot(p[i], doe[i], 0, 0) for i in n]
            dk_part = [_dot(ds[i], jnp.where(masks[heads[i][1]], qa[i], jnp.zeros_like(qa[i])), 0, 0) for i in n]
            dq_part = [jnp.where(masks[heads[i][1]], _dot(ds[i], kas[i]), 0.0) for i in n]
            css = tuple(css[i] + jnp.sum(ds[i], axis=0, keepdims=True) for i in n)
            dc = jnp.zeros((BQ, LANES), F32)
            for i in n:
                dc = dc + jnp.where(lane == head_index(*heads[i]), jnp.sum(ds[i], axis=1, keepdims=True), 0.0)
            for pp in range(pg):
                dq_ref[pp, rows, :] += (dq_part[2 * pp] + dq_part[2 * pp + 1]) * FOX_SCALE
            dc_ref[rows, :] += dc
            dks = tuple(dks[pp] + dk_part[2 * pp] + dk_part[2 * pp + 1] for pp in range(pg))
            dvs = tuple(dvs[pp] + dv_part[2 * pp] + dv_part[2 * pp + 1] for pp in range(pg))
            return dks, dvs, css

        zero = jnp.zeros((BQ, LANES), F32)
        first = block(kj, ((zero,) * pg, (zero,) * pg, (jnp.zeros((1, BQ), F32),) * len(heads)), True)
        dks, dvs, css = lax.fori_loop(kj + 1, NQ, lambda qi, carry: block(qi, carry, False), first)
        r = lax.broadcasted_iota(jnp.int32, (BQ, BQ), 0)
        c = lax.broadcasted_iota(jnp.int32, (BQ, BQ), 1)
        dcol = jnp.zeros((BQ, LANES), F32)
        for i in n:
            col = jnp.sum(jnp.where(r == c, css[i], 0.0), axis=1, keepdims=True)
            dcol = dcol + jnp.where(lane == head_index(*heads[i]), col, 0.0)
        dc_ref[krows, :] -= dcol
        for pp in range(pg):
            base = 3 * pp * LANES
            dproj_ref[krows, base + LANES:base + 2 * LANES] = dks[pp].astype(BF16)
            dproj_ref[krows, base + 2 * LANES:base + 3 * LANES] = dvs[pp].astype(BF16)

        @pl.when(kj == NQ - 1)
        def _():
            for pp in range(pg):
                dproj_ref[:, 3 * pp * LANES:(3 * pp + 1) * LANES] = dq_ref[pp].astype(BF16)

        if exchange is not None:
            @pl.when((grp == NPAIR // pg // 2) & (kj == 0))
            def _():
                exchange.middle(*exchange.split(ex_refs))

            @pl.when((grp == NPAIR // pg - 1) & (kj == NQ - 1))
            def _():
                exchange.rest(*exchange.split(ex_refs))

    qkv_specs = []
    for pp in range(pg):
        qkv_specs.append(pl.BlockSpec((S, LANES), lambda g, j, pp=pp: (0, 3 * (pg * g + pp))))
        qkv_specs.append(pl.BlockSpec((BQ, LANES), lambda g, j, pp=pp: (j, 3 * (pg * g + pp) + 1)))
        qkv_specs.append(pl.BlockSpec((BQ, LANES), lambda g, j, pp=pp: (j, 3 * (pg * g + pp) + 2)))
    pair = pl.BlockSpec((pg, S, 2), lambda g, j: (g, 0, 0))
    res = pl.pallas_call(
        body, name="fox_bwd", grid=(NPAIR // pg, NQ),
        in_specs=qkv_specs + [pl.BlockSpec((S, pg * LANES), lambda g, j: (0, g)), pl.BlockSpec((S, LANES), lambda g, j: (0, 0)),
                              pair, pair] + ex_in_specs,
        out_specs=[pl.BlockSpec((S, 3 * pg * LANES), lambda g, j: (0, g)), pl.BlockSpec((S, LANES), lambda g, j: (0, 0))]
        + ex_out_specs,
        out_shape=[jax.ShapeDtypeStruct((S, DPROJ_PAD), BF16), jax.ShapeDtypeStruct((S, LANES), F32)] + ex_out_shape,
        scratch_shapes=[pltpu.VMEM((2 * pg, S, LANES), BF16), pltpu.VMEM((pg, S, LANES), F32)] + ex_scratch,
        compiler_params=_cparams(),
    )(*([proj] * (3 * pg)), do, gates, lse, delta, *ex_in)
    return res[0], res[1], res[2:]


NQKV = 3 * NGH
GDN_QSCALE = GHD ** -0.5


def _shift_down(x, s):
    if s == 0:
        return x
    row = lax.broadcasted_iota(jnp.int32, x.shape, 0)
    return jnp.where(row >= s, pltpu.roll(x, s, 0), 0.0)


def _shift_up(x, s):
    if s == 0:
        return x
    n = x.shape[0]
    row = lax.broadcasted_iota(jnp.int32, x.shape, 0)
    return jnp.where(row < n - s, pltpu.roll(x, n - s, 0), 0.0)


def _conv_taps(xv):
    return [_shift_down(xv, CONV_K - 1 - j) for j in range(CONV_K)]


def _conv_pre(taps, wv):
    pre = taps[CONV_K - 1] * wv[CONV_K - 1:CONV_K, :]
    for j in range(CONV_K - 1):
        pre = pre + taps[j] * wv[j:j + 1, :]
    return pre


def _l2_factors(b):
    return b < 2 * NGH, jnp.where(b < NGH, GDN_QSCALE, 1.0)


def _gdn_pre(proj, conv_w):
    def body(x_ref, w_ref, o_ref):
        b = pl.program_id(0)
        c = _silu(_conv_pre(_conv_taps(x_ref[...]), w_ref[...]))
        normed, scale = _l2_factors(b)
        rs = lax.rsqrt(jnp.sum(c * c, axis=-1, keepdims=True) + EPS)
        o_ref[...] = c * jnp.where(normed, rs, 1.0) * scale

    return pl.pallas_call(
        body, name="gdn_pre", grid=(NQKV,),
        in_specs=[pl.BlockSpec((S, GHD), lambda b: (0, BLK_GDN + b)), pl.BlockSpec((CONV_K, GHD), lambda b: (0, b))],
        out_specs=pl.BlockSpec((S, GHD), lambda b: (0, b)),
        out_shape=jax.ShapeDtypeStruct((S, NQKV * GHD), F32), compiler_params=_cparams(),
    )(proj, conv_w)


def _gdn_pre_bwd(proj, conv_w, dqkv, dproj):
    def body(x_ref, w_ref, dy_ref, dproj_in, dx_ref, dw_ref):
        del dproj_in
        b = pl.program_id(0)
        taps = _conv_taps(x_ref[...])
        wv = w_ref[...]
        pre = _conv_pre(taps, wv)
        sig = _sigmoid(pre)
        c = pre * sig
        normed, scale = _l2_factors(b)
        g = dy_ref[...] * scale
        rs = lax.rsqrt(jnp.sum(c * c, axis=-1, keepdims=True) + EPS)
        dc_n = rs * g - c * (rs * rs * rs) * jnp.sum(g * c, axis=-1, keepdims=True)
        dc = jnp.where(normed, dc_n, g)
        dpre = dc * sig * (1.0 + pre * (1.0 - sig))
        dx = dpre * wv[CONV_K - 1:CONV_K, :]
        for j in range(CONV_K - 1):
            dx = dx + _shift_up(dpre, CONV_K - 1 - j) * wv[j:j + 1, :]
        dx_ref[...] = dx.astype(BF16)
        for j in range(CONV_K):
            dw_ref[j:j + 1, :] = jnp.sum(dpre * taps[j], axis=0, keepdims=True)

    return pl.pallas_call(
        body, name="gdn_pre_bwd", grid=(NQKV,),
        in_specs=[pl.BlockSpec((S, GHD), lambda b: (0, BLK_GDN + b)), pl.BlockSpec((CONV_K, GHD), lambda b: (0, b)),
                  pl.BlockSpec((None, S, GHD), lambda b: (b // NGH, 0, b % NGH)), pl.BlockSpec(memory_space=pl.ANY)],
        out_specs=[pl.BlockSpec((S, GHD), lambda b: (0, BLK_GDN + b)), pl.BlockSpec((CONV_K, GHD), lambda b: (0, b))],
        out_shape=[jax.ShapeDtypeStruct((S, DPROJ_PAD), BF16), jax.ShapeDtypeStruct((CONV_K, NQKV * GHD), F32)],
        input_output_aliases={3: 0}, compiler_params=_cparams(),
    )(proj, conv_w, dqkv, dproj)


CB = 16
NCB = NCH // CB


def _chunk_prep(qs, ks, vs, gcols, bcols, t_saved=None):
    n = range(len(qs))
    r = lax.broadcasted_iota(jnp.int32, (CHUNK, CHUNK), 0)
    c = lax.broadcasted_iota(jnp.int32, (CHUNK, CHUNK), 1)
    incl = c <= r
    eye = (r == c).astype(F32)
    grow = [jnp.sum(gcols[i] * eye, axis=0, keepdims=True) for i in n]
    gc_col = [jnp.sum(jnp.where(incl, grow[i], 0.0), axis=1, keepdims=True) for i in n]
    gc_row = [jnp.sum(jnp.where(r <= c, gcols[i], 0.0), axis=0, keepdims=True) for i in n]
    decay = [jnp.exp(jnp.where(incl, gc_col[i] - gc_row[i], -jnp.inf)) for i in n]
    kb = [ks[i] * bcols[i] for i in n]
    vb = [vs[i] * bcols[i] for i in n]
    kk = [_mm_nt(kb[i], ks[i]) for i in n]
    m = [jnp.where(c < r, kk[i] * decay[i], 0.0) for i in n]
    if t_saved is None:
        t_inv = [eye - m[i] for i in n]
        p = [_dot3(m[i], m[i]) for i in n]
        for step in range(5):
            t_inv = [t_inv[i] + _dot3(t_inv[i], p[i]) for i in n]
            if step < 4:
                p = [_dot3(p[i], p[i]) for i in n]
    else:
        t_inv = [_saved_inverse(m[i], t_saved[i]) for i in n]
    egc = [jnp.exp(gc_col[i]) for i in n]
    u = [_mm_nn(t_inv[i], vb[i]) for i in n]
    w = [_mm_nn(t_inv[i], kb[i] * egc[i]) for i in n]
    qk = [_mm_nt(qs[i], ks[i]) for i in n]
    gc_last = [gc_col[i][CHUNK - 1:CHUNK, :] for i in n]
    return [(u[i], w[i], qk[i] * decay[i], qs[i] * egc[i], ks[i] * jnp.exp(gc_last[i] - gc_col[i]), jnp.exp(gc_last[i]),
             t_inv[i]) for i in n]


def _prep_specs():
    rows = CB * CHUNK
    qs = pl.BlockSpec((rows, GHD), lambda i, h: (i, h))
    ks = pl.BlockSpec((rows, GHD), lambda i, h: (i, NGH + h))
    vs = pl.BlockSpec((rows, GHD), lambda i, h: (i, 2 * NGH + h))
    gs = pl.BlockSpec((rows, LANES), lambda i, h: (i, 0))
    a_s = pl.BlockSpec((None, rows, CHUNK), lambda i, h: (h, i, 0))
    gl_s = pl.BlockSpec((None, CB, 1, LANES), lambda i, h: (h, i, 0, 0))
    return qs, ks, vs, gs, a_s, gl_s


def _gdn_prep(qkv, gates, exchange=None):
    ex_in, ex_in_specs, ex_out_specs, ex_out_shape, ex_scratch = _hosted(exchange)

    def body(*refs):
        q_ref, k_ref, v_ref, g_ref = refs[:4]
        u_ref, w_ref, qd_ref, kd_ref, a_ref, gl_ref, t_ref = refs[4 + len(ex_in):11 + len(ex_in)]
        ex_refs = refs[4:4 + len(ex_in)] + refs[11 + len(ex_in):]
        h = pl.program_id(1)

        if exchange is not None:
            @pl.when((pl.program_id(0) == 0) & (h == 0))
            def _():
                exchange.start(*exchange.split(ex_refs))

        chunks = [pl.ds(cidx * CHUNK, CHUNK) for cidx in range(CB)]
        gts = [g_ref[rows, :] for rows in chunks]
        outs = _chunk_prep([q_ref[rows, :] for rows in chunks], [k_ref[rows, :] for rows in chunks],
                           [v_ref[rows, :] for rows in chunks], [_lane_col(gt, LANE_G + h) for gt in gts],
                           [_lane_col(gt, LANE_BETA + h) for gt in gts])
        for cidx, rows in enumerate(chunks):
            u, w, a, qd, kd, gl, t_inv = outs[cidx]
            u_ref[rows, :] = u
            w_ref[rows, :] = w
            qd_ref[rows, :] = qd
            kd_ref[rows, :] = kd
            a_ref[rows, :] = a
            t_ref[rows, :] = t_inv
            gl_ref[cidx] = jnp.broadcast_to(gl, (1, LANES))

        if exchange is not None:
            @pl.when((pl.program_id(0) == NCB // 2) & (h == 0))
            def _():
                exchange.middle(*exchange.split(ex_refs))

            @pl.when((pl.program_id(0) == NCB - 1) & (h == NGH - 1))
            def _():
                exchange.rest(*exchange.split(ex_refs))

    qs, ks, vs, gs, a_s, gl_s = _prep_specs()
    tok = jax.ShapeDtypeStruct((S, DGDN), F32)
    sq = jax.ShapeDtypeStruct((NGH, S, CHUNK), F32)
    res = pl.pallas_call(
        body, name="gdn_prep", grid=(NCB, NGH), in_specs=[qs, ks, vs, gs] + ex_in_specs,
        out_specs=[qs, qs, qs, qs, a_s, gl_s, a_s] + ex_out_specs,
        out_shape=[tok, tok, tok, tok, sq, jax.ShapeDtypeStruct((NGH, NCH, 1, LANES), F32), sq] + ex_out_shape,
        scratch_shapes=ex_scratch, compiler_params=_cparams(),
    )(qkv, qkv, qkv, gates, *ex_in)
    return res[:7], res[7:]


def _gdn_prep_bwd(qkv, gates, t_inv, du, dw, dqd, dkd, da, dgl, exchange=None):
    ex_in, ex_in_specs, ex_out_specs, ex_out_shape, ex_scratch = _hosted(exchange)

    def body(*refs):
        q_ref, k_ref, v_ref, g_ref, t_ref, du_ref, dw_ref, dqd_ref, dkd_ref, da_ref, dgl_ref = refs[:11]
        dqkv_ref, dg_ref = refs[11 + len(ex_in):13 + len(ex_in)]
        ex_refs = refs[11:11 + len(ex_in)] + refs[13 + len(ex_in):]
        h = pl.program_id(1)

        if exchange is not None:
            @pl.when((pl.program_id(0) == 0) & (h == 0))
            def _():
                exchange.start(*exchange.split(ex_refs))

        @pl.when(h == 0)
        def _():
            dg_ref[...] = jnp.zeros_like(dg_ref)

        lane = lax.broadcasted_iota(jnp.int32, (CHUNK, LANES), 1)
        chunks = [pl.ds(cidx * CHUNK, CHUNK) for cidx in range(CB)]
        gts = [g_ref[rows, :] for rows in chunks]
        t_saved = [t_ref[rows, :] for rows in chunks]
        _, vjp = jax.vjp(lambda *args: [o[:6] for o in _chunk_prep(*args, t_saved=t_saved)],
                         [q_ref[rows, :] for rows in chunks], [k_ref[rows, :] for rows in chunks],
                         [v_ref[rows, :] for rows in chunks], [_lane_col(gt, LANE_G + h) for gt in gts],
                         [_lane_col(gt, LANE_BETA + h) for gt in gts])
        dqs, dks, dvs, dgcs, dbcs = vjp([(du_ref[rows, :], dw_ref[rows, :], da_ref[rows, :], dqd_ref[rows, :],
                                          dkd_ref[rows, :], dgl_ref[cidx][:, 0:1]) for cidx, rows in enumerate(chunks)])
        for cidx, rows in enumerate(chunks):
            dq, dk, dv, dgc, dbc = dqs[cidx], dks[cidx], dvs[cidx], dgcs[cidx], dbcs[cidx]
            dqkv_ref[0, rows, :] = dq
            dqkv_ref[1, rows, :] = dk
            dqkv_ref[2, rows, :] = dv
            dg_ref[rows, :] += jnp.where(lane == LANE_G + h, dgc, 0.0) + jnp.where(lane == LANE_BETA + h, dbc, 0.0)

        if exchange is not None:
            @pl.when((pl.program_id(0) == NCB - 1) & (h == NGH - 1))
            def _():
                exchange.finish(*exchange.split(ex_refs))

    qs, ks, vs, gs, a_s, gl_s = _prep_specs()
    res = pl.pallas_call(
        body, name="gdn_prep_bwd", grid=(NCB, NGH), in_specs=[qs, ks, vs, gs, a_s, qs, qs, qs, qs, a_s, gl_s] + ex_in_specs,
        out_specs=[pl.BlockSpec((3, CB * CHUNK, GHD), lambda i, h: (0, i, h)), gs] + ex_out_specs,
        out_shape=[jax.ShapeDtypeStruct((3, S, DGDN), F32), jax.ShapeDtypeStruct((S, LANES), F32)] + ex_out_shape,
        scratch_shapes=ex_scratch, compiler_params=_cparams(),
    )(qkv, qkv, qkv, gates, t_inv, du, dw, dqd, dkd, da, dgl, *ex_in)
    return res[0], res[1], res[2:]


def _scan_specs(nh, parts, reverse):
    wide, rows, chunks = nh * GHD, S // parts, NCH // parts

    def part(p):
        return parts - 1 - p if reverse else p

    hs = pl.BlockSpec((rows, wide), lambda g, p: (part(p), g))
    a_s = pl.BlockSpec((nh, rows, CHUNK), lambda g, p: (g, part(p), 0))
    gl_s = pl.BlockSpec((nh, chunks, 1, LANES), lambda g, p: (g, part(p), 0, 0))
    st_s = pl.BlockSpec((nh, chunks, GHD, GHD), lambda g, p: (g, part(p), 0, 0))
    gz_s = pl.BlockSpec((rows, wide), lambda g, p: (part(p), BLK_GZ // nh + g))
    mix_s = pl.BlockSpec((rows, wide), lambda g, p: (part(p), NPAIR // nh + g))
    return hs, a_s, gl_s, st_s, gz_s, mix_s


def _head_cols(hh):
    return slice(hh * GHD, (hh + 1) * GHD)


SCAN_HEADS, SCAN_PARTS = 4, 2
SCAN_HEADS_BWD, SCAN_PARTS_BWD = 4, 4


def _gdn_scan(u, w, qd, kd, a, gl, proj, w_norm, mix):
    heads = range(SCAN_HEADS)

    def body(u_ref, w_ref, qd_ref, kd_ref, a_ref, gl_ref, z_ref, wn_ref, mix_in, mix_ref, o_ref, st_ref, carry_ref):
        del mix_in

        @pl.when(pl.program_id(1) == 0)
        def _():
            carry_ref[...] = jnp.zeros_like(carry_ref)

        def step(ci, states):
            rows = pl.ds(pl.multiple_of(ci * CHUNK, CHUNK), CHUNK)
            for hh in heads:
                st_ref[hh, ci] = states[hh]
            ws = [_dot(w_ref[rows, _head_cols(hh)], states[hh]) for hh in heads]
            qs = [_dot(qd_ref[rows, _head_cols(hh)], states[hh]) for hh in heads]
            vn = [u_ref[rows, _head_cols(hh)] - ws[hh] for hh in heads]
            av = [_dot(a_ref[hh, rows, :], vn[hh]) for hh in heads]
            kv = [_dot(kd_ref[rows, _head_cols(hh)], vn[hh], 0, 0) for hh in heads]
            for hh in heads:
                o_ref[rows, _head_cols(hh)] = qs[hh] + av[hh]
            return tuple(states[hh] * gl_ref[hh, ci] + kv[hh] for hh in heads)

        last = lax.fori_loop(0, NCH // SCAN_PARTS, step, tuple(carry_ref[hh] for hh in heads))
        for hh in heads:
            carry_ref[hh] = last[hh]
            ov = o_ref[:, _head_cols(hh)]
            mix_ref[:, _head_cols(hh)] = (ov * _rms_scale(ov) * wn_ref[...] * _silu(z_ref[:, _head_cols(hh)])).astype(BF16)

    hs, a_s, gl_s, st_s, gz_s, mix_s = _scan_specs(SCAN_HEADS, SCAN_PARTS, False)
    return pl.pallas_call(
        body, name="gdn_scan", grid=(NGH // SCAN_HEADS, SCAN_PARTS),
        in_specs=[hs, hs, hs, hs, a_s, gl_s, gz_s, pl.BlockSpec((1, GHD), lambda g, p: (0, 0)),
                  pl.BlockSpec(memory_space=pl.ANY)],
        out_specs=[mix_s, hs, st_s],
        out_shape=[jax.ShapeDtypeStruct((S, D), BF16), jax.ShapeDtypeStruct((S, DGDN), F32),
                   jax.ShapeDtypeStruct((NGH, NCH, GHD, GHD), F32)],
        input_output_aliases={8: 0}, scratch_shapes=[pltpu.VMEM((SCAN_HEADS, GHD, GHD), F32)], compiler_params=_cparams(),
    )(u, w, qd, kd, a, gl, proj, w_norm, mix)


def _gdn_scan_bwd(dmix, o, proj, w_norm, u, w, qd, kd, a, gl, states, dproj, exchange=None):
    ex_in, ex_in_specs, ex_out_specs, ex_out_shape, ex_scratch = _hosted(exchange)
    groups = NGH // SCAN_HEADS_BWD

    def body(*refs):
        dy_ref, o_ref, z_ref, wn_ref, u_ref, w_ref, qd_ref, kd_ref, a_ref, gl_ref, st_ref = refs[:11]
        dz_ref, du_ref, dw_ref, dqd_ref, dkd_ref, da_ref, dgl_ref, dwn_ref = refs[12 + len(ex_in):20 + len(ex_in)]
        do_ref, carry_ref = refs[20 + len(ex_in) + len(ex_out_shape):22 + len(ex_in) + len(ex_out_shape)]
        ex_refs = refs[12:12 + len(ex_in)] + refs[20 + len(ex_in):20 + len(ex_in) + len(ex_out_shape)] + refs[-2:]
        heads = range(SCAN_HEADS_BWD)
        chunks = NCH // SCAN_PARTS_BWD

        if exchange is not None:
            @pl.when((pl.program_id(0) == 0) & (pl.program_id(1) == 0))
            def _():
                exchange.start(*exchange.split(ex_refs))

        @pl.when((pl.program_id(0) == 0) & (pl.program_id(1) == 0))
        def _():
            dwn_ref[...] = jnp.zeros_like(dwn_ref)

        @pl.when(pl.program_id(1) == 0)
        def _():
            carry_ref[...] = jnp.zeros_like(carry_ref)

        wn = wn_ref[...]
        for hh in heads:
            c = _head_cols(hh)
            ov = o_ref[:, c]
            zv = z_ref[:, c]
            g = dy_ref[:, c]
            sig = _sigmoid(zv)
            dz_ref[:, c] = (g * (ov * _rms_scale(ov) * wn) * sig * (1.0 + zv * (1.0 - sig))).astype(BF16)
            do, dwt = _rms_bwd(ov, wn, g * zv * sig)
            do_ref[:, c] = do
            dwn_ref[...] += jnp.sum(dwt, axis=0, keepdims=True)

        def step(t, dstates):
            ci = chunks - 1 - t
            rows = pl.ds(pl.multiple_of(ci * CHUNK, CHUNK), CHUNK)
            cols = [_head_cols(hh) for hh in heads]
            state = [st_ref[hh, ci] for hh in heads]
            dov = [do_ref[rows, cols[hh]] for hh in heads]
            wv = [w_ref[rows, cols[hh]] for hh in heads]
            ws = [_dot(wv[hh], state[hh]) for hh in heads]
            adov = [_dot(a_ref[hh, rows, :], dov[hh], 0, 0) for hh in heads]
            kds = [_dot(kd_ref[rows, cols[hh]], dstates[hh]) for hh in heads]
            dqd = [_dot(dov[hh], state[hh], 1, 1) for hh in heads]
            qdo = [_dot(qd_ref[rows, cols[hh]], dov[hh], 0, 0) for hh in heads]
            vn = [u_ref[rows, cols[hh]] - ws[hh] for hh in heads]
            dvn = [adov[hh] + kds[hh] for hh in heads]
            da = [_dot(dov[hh], vn[hh], 1, 1) for hh in heads]
            dkd = [_dot(vn[hh], dstates[hh], 1, 1) for hh in heads]
            dwv = [_dot(dvn[hh], state[hh], 1, 1) for hh in heads]
            wdv = [_dot(wv[hh], dvn[hh], 0, 0) for hh in heads]
            for hh in heads:
                da_ref[hh, rows, :] = da[hh]
                dqd_ref[rows, cols[hh]] = dqd[hh]
                dkd_ref[rows, cols[hh]] = dkd[hh]
                dgl = jnp.sum(jnp.sum(dstates[hh] * state[hh], axis=1, keepdims=True), axis=0, keepdims=True)
                dgl_ref[hh, ci] = jnp.broadcast_to(dgl, (1, LANES))
                du_ref[rows, cols[hh]] = dvn[hh]
                dw_ref[rows, cols[hh]] = -dwv[hh]
            return tuple(dstates[hh] * gl_ref[hh, ci] + qdo[hh] - wdv[hh] for hh in heads)

        last = lax.fori_loop(0, chunks, step, tuple(carry_ref[hh] for hh in heads))
        for hh in heads:
            carry_ref[hh] = last[hh]

        if exchange is not None:
            @pl.when((pl.program_id(0) == groups - 1) & (pl.program_id(1) == SCAN_PARTS_BWD - 1))
            def _():
                exchange.finish(*exchange.split(ex_refs))

    hs, a_s, gl_s, st_s, gz_s, mix_s = _scan_specs(SCAN_HEADS_BWD, SCAN_PARTS_BWD, True)
    vec = pl.BlockSpec((1, GHD), lambda g, p: (0, 0))
    tok = jax.ShapeDtypeStruct((S, DGDN), F32)
    res = pl.pallas_call(
        body, name="gdn_scan_bwd", grid=(groups, SCAN_PARTS_BWD),
        in_specs=[mix_s, hs, gz_s, vec, hs, hs, hs, hs, a_s, gl_s, st_s, pl.BlockSpec(memory_space=pl.ANY)] + ex_in_specs,
        out_specs=[gz_s, hs, hs, hs, hs, a_s, gl_s, vec] + ex_out_specs,
        out_shape=[jax.ShapeDtypeStruct((S, DPROJ_PAD), BF16), tok, tok, tok, tok,
                   jax.ShapeDtypeStruct((NGH, S, CHUNK), F32), jax.ShapeDtypeStruct((NGH, NCH, 1, LANES), F32),
                   jax.ShapeDtypeStruct((1, GHD), F32)] + ex_out_shape,
        input_output_aliases={11: 0},
        scratch_shapes=[pltpu.VMEM((S // SCAN_PARTS_BWD, SCAN_HEADS_BWD * GHD), F32),
                        pltpu.VMEM((SCAN_HEADS_BWD, GHD, GHD), F32)] + ex_scratch,
        compiler_params=_cparams(),
    )(dmix, o, proj, w_norm, u, w, qd, kd, a, gl, states, dproj, *ex_in)
    return res[:8], res[8:]


def _place():
    return lax.axis_index("x"), lax.axis_index("y"), lax.axis_index("c")


def _place_scalars():
    x, y, c = _place()
    return jnp.stack([2 * x + y, c]).astype(jnp.int32)


def _other_chips(x, y):
    return [(1 - x, y), (x, 1 - y), (1 - x, 1 - y)]


HBM = pl.BlockSpec(memory_space=pltpu.HBM)
VMEM = pl.BlockSpec(memory_space=pltpu.VMEM)


def _half_rows(ref_or_rows, half):
    rows = ref_or_rows // 2
    return pl.ds(pl.multiple_of(half * rows, rows), rows)


class _Exchange:
    def __init__(self, inputs, out_shape, n_sems, start, finish=None, middle=None, rest=None):
        self.inputs, self.out_shape, self.n_sems, self.start = inputs, out_shape, n_sems, start
        if finish is None:
            def finish(*refs):
                middle(*refs)
                rest(*refs)
        self.finish = finish
        self.middle = middle if middle is not None else (lambda *refs: None)
        self.rest = rest if rest is not None else finish

    def sem_shapes(self):
        return [pltpu.SemaphoreType.DMA((self.n_sems,)), pltpu.SemaphoreType.DMA((self.n_sems,))]

    def split(self, refs):
        n_in, n_out = len(self.inputs), len(self.out_shape)
        return refs[:n_in], refs[n_in:n_in + n_out], refs[n_in + n_out], refs[n_in + n_out + 1]


def _allgather_exchange(shards, whole=()):
    n, nw = len(shards), len(whole)
    slots = 8

    def plan(src, outs, send_sems, recv_sems):
        x, y, c = _place()
        via_x, via_y, diagonal = _other_chips(x, y)
        id_x, id_y, id_diagonal = [2 * chip[0] + chip[1] for chip in (via_x, via_y, diagonal)]
        me, sibling = (x, y, c), (x, y, 1 - c)

        def rows_of(a, half, quarter):
            total = src[a].shape[0]
            if quarter is None:
                return _half_rows(total, half)
            return pl.ds(pl.multiple_of(half * (total // 2) + quarter * (total // 4), total // 4), total // 4)

        def copy(a, k, chip_index, half, quarter, to, from_src=False):
            rows = rows_of(a, half, quarter)
            dst = outs[a].at[chip_index, rows]
            return pltpu.make_async_remote_copy(
                src_ref=src[a].at[rows] if from_src else dst, dst_ref=dst, send_sem=send_sems.at[slots * a + k],
                recv_sem=recv_sems.at[slots * a + k], device_id=to, device_id_type=MESH)

        def whole_copy(b, k, chip_index, to):
            return pltpu.make_async_remote_copy(
                src_ref=src[n + b], dst_ref=outs[n + b].at[chip_index], send_sem=send_sems.at[slots * n + 3 * b + k],
                recv_sem=recv_sems.at[slots * n + 3 * b + k], device_id=to, device_id_type=MESH)

        first, stages, last = [], [], []
        for a in range(n):
            first += [copy(a, 0, 2 * x + y, c, None, (*via_x, c), True), copy(a, 1, 2 * x + y, c, None, (*via_y, c), True)]
            stages.append([
                (copy(a, 0, id_x, c, None, me),
                 [copy(a, 2, id_x, c, 0, (*via_y, c)), copy(a, 4, id_x, c, None, sibling)]),
                (copy(a, 1, id_y, c, None, me),
                 [copy(a, 3, id_y, c, 1, (*via_x, c)), copy(a, 5, id_y, c, None, sibling)]),
                (copy(a, 2, id_diagonal, c, 0, me), [copy(a, 6, id_diagonal, c, 0, sibling)]),
                (copy(a, 3, id_diagonal, c, 1, me), [copy(a, 7, id_diagonal, c, 1, sibling)]),
            ])
            last += [copy(a, 4, id_x, 1 - c, None, me), copy(a, 5, id_y, 1 - c, None, me),
                     copy(a, 6, id_diagonal, 1 - c, 0, me), copy(a, 7, id_diagonal, 1 - c, 1, me)]
        for b in range(nw):
            for k, (chip, index) in enumerate(((via_x, id_x), (via_y, id_y), (diagonal, id_diagonal))):
                first.append(whole_copy(b, k, 2 * x + y, (*chip, c)))
                last.append(whole_copy(b, k, index, me))
        return first, stages, last

    def start(*refs):
        for cp in plan(*refs)[0]:
            cp.start()

    def pass_on(stages, which):
        for stage in which:
            for per_shard in stages:
                lands, onward = per_shard[stage]
                lands.wait_recv()
                for cp in onward:
                    cp.start()

    def middle(*refs):
        pass_on(plan(*refs)[1], (0, 1))

    def rest(*refs):
        first, stages, last = plan(*refs)
        pass_on(stages, (2, 3))
        for cp in last:
            cp.wait_recv()
        for cp in first + [cp for per_shard in stages for _, onward in per_shard for cp in onward]:
            cp.wait_send()

    out_shape = [jax.ShapeDtypeStruct((NCHIP,) + s.shape, s.dtype) for s in list(shards) + list(whole)]
    return _Exchange(list(shards) + list(whole), out_shape, slots * n + 3 * nw, start, middle=middle, rest=rest)


def _with_own(gathered, own):
    x, y, _ = _place()
    return lax.dynamic_update_index_in_dim(gathered, own, 2 * x + y, axis=0)


def _simple_exchange(inputs, out_shape, copies_of):
    def start(*refs):
        for cp in copies_of(*refs):
            cp.start()

    def finish(*refs):
        for cp in copies_of(*refs):
            cp.wait()

    return _Exchange(list(inputs), out_shape, len(out_shape) * 3, start, finish)


def _pair_exchange(grads):
    def copies_of(src, outs, send_sems, recv_sems):
        x, y, c = _place()
        return [pltpu.make_async_remote_copy(
            src_ref=src[a].at[:, _half_rows(src[a].shape[1], 1 - c)], dst_ref=outs[a], send_sem=send_sems.at[a],
            recv_sem=recv_sems.at[a], device_id=(x, y, 1 - c), device_id_type=MESH) for a in range(len(src))]

    return _simple_exchange(
        grads, [jax.ShapeDtypeStruct((g.shape[0], g.shape[1] // 2, g.shape[2]), g.dtype) for g in grads], copies_of)


def _pair_sum(grads, theirs, name):
    n = len(grads)

    def body(place_ref, *refs):
        for a in range(n):
            refs[2 * n + a][...] = (refs[a][...].astype(F32) + refs[n + a][...].astype(F32)).astype(BF16)

    def specs(arrs):
        return [pl.BlockSpec((None,) + g.shape[1:], lambda j, place: (j, 0, 0)) for g in arrs]

    own_half = [pl.BlockSpec((None, g.shape[1] // 2, g.shape[2]), lambda j, place: (j, place[1], 0)) for g in grads]
    return pl.pallas_call(
        body, name=name, grid_spec=pltpu.PrefetchScalarGridSpec(
            num_scalar_prefetch=1, grid=(NCHIP,), in_specs=own_half + specs(theirs), out_specs=specs(theirs)),
        out_shape=[jax.ShapeDtypeStruct(g.shape, BF16) for g in theirs], compiler_params=_cparams(),
    )(_place_scalars(), *grads, *theirs)


def _chip_exchange(parts):
    def copies_of(src, outs, send_sems, recv_sems):
        x, y, c = _place()
        return [pltpu.make_async_remote_copy(
            src_ref=src[a].at[2 * chip[0] + chip[1]], dst_ref=outs[a].at[k], send_sem=send_sems.at[3 * a + k],
            recv_sem=recv_sems.at[3 * a + k], device_id=(*chip, c), device_id_type=MESH)
            for a in range(len(src)) for k, chip in enumerate(_other_chips(x, y))]

    return _simple_exchange(parts, [jax.ShapeDtypeStruct((NCHIP - 1,) + p.shape[1:], p.dtype) for p in parts], copies_of)


def _chip_sum(parts, received, exchange=None):
    n = len(parts)
    steps = 4
    ex_in, ex_in_specs, ex_out_specs, ex_out_shape, ex_scratch = _hosted(exchange)
    n_ex = len(ex_in)

    def body(place_ref, *refs):
        mine, theirs = refs[2 * n + n_ex:3 * n + n_ex], refs[3 * n + n_ex:4 * n + n_ex]
        ex_refs = refs[2 * n:2 * n + n_ex] + refs[4 * n + n_ex:len(refs) - n - 2]
        tiles, send_sems, recv_sems = refs[len(refs) - n - 2:len(refs) - 2], refs[-2], refs[-1]
        step = pl.program_id(0)
        if exchange is not None:
            @pl.when(step == 0)
            def _():
                exchange.start(*exchange.split(ex_refs))

        x, y, c = _place()

        def share(a, i):
            rows = tiles[a].shape[1]
            return pltpu.make_async_remote_copy(
                src_ref=tiles[a].at[i], dst_ref=theirs[a].at[pl.ds(pl.multiple_of(i * rows, rows), rows)],
                send_sem=send_sems.at[a * steps + i], recv_sem=recv_sems.at[a * steps + i], device_id=(x, y, 1 - c),
                device_id_type=MESH)

        for a in range(n):
            own, r = refs[a], refs[n + a]
            total = ((own[...].astype(F32) + r[0].astype(F32)) + r[1].astype(F32)) + r[2].astype(F32)
            mine[a][...] = total
            tiles[a][step] = total
            share(a, step).start()

        @pl.when(step == steps - 1)
        def _():
            if exchange is not None:
                exchange.finish(*exchange.split(ex_refs))
            for a in range(n):
                for i in range(steps):
                    share(a, i).wait()

    own_specs = [pl.BlockSpec((None, g.shape[1] // steps, g.shape[2]), lambda i, place: (place[0], i, 0)) for g in parts]
    received_specs = [pl.BlockSpec((g.shape[0], g.shape[1] // steps, g.shape[2]), lambda i, place: (0, i, 0))
                      for g in received]
    out_specs = [pl.BlockSpec((g.shape[1] // steps, g.shape[2]), lambda i, place: (i, 0)) for g in parts]
    halves = [jax.ShapeDtypeStruct(g.shape[1:], F32) for g in parts]
    res = pl.pallas_call(
        body, name="grads_chip_sum", grid_spec=pltpu.PrefetchScalarGridSpec(
            num_scalar_prefetch=1, grid=(steps,), in_specs=own_specs + received_specs + ex_in_specs,
            out_specs=out_specs + [HBM] * n + ex_out_specs,
            scratch_shapes=ex_scratch + [pltpu.VMEM((steps, g.shape[1] // steps, g.shape[2]), F32) for g in parts]
            + [pltpu.SemaphoreType.DMA((n * steps,))] * 2),
        out_shape=halves * 2 + ex_out_shape, compiler_params=_cparams(),
    )(_place_scalars(), *parts, *received, *ex_in)
    return res[:n], res[n:2 * n], res[2 * n:]


def _adamw_math(w, g, m, v):
    nm = ADAM_B1 * m + (1.0 - ADAM_B1) * g
    nv = ADAM_B2 * v + (1.0 - ADAM_B2) * jnp.square(g)
    m_hat = nm / (1.0 - ADAM_B1 ** ADAM_STEP)
    v_hat = nv / (1.0 - ADAM_B2 ** ADAM_STEP)
    return -ADAM_LR * (m_hat / (jnp.sqrt(v_hat) + ADAM_EPS) + ADAM_WD * w), nm, nv


def _adamw_big(ws, g_mine, g_theirs, ms, vs, exchange=None):
    n = len(ws)
    steps = 8
    ex_in, ex_in_specs, ex_out_specs, ex_out_shape, ex_scratch = _hosted(exchange)

    def body(*refs):
        ex_refs = refs[5 * n:5 * n + len(ex_in)] + refs[9 * n + len(ex_in):]
        outs = refs[5 * n + len(ex_in):9 * n + len(ex_in)]
        if exchange is not None:
            @pl.when(pl.program_id(0) == 0)
            def _():
                exchange.start(*exchange.split(ex_refs))

        own_half = (pl.program_id(0) // (steps // 2)) == lax.axis_index("c")
        for a in range(n):
            g = jnp.where(own_half, refs[n + a][...], refs[2 * n + a][...])
            d, nm, nv = _adamw_math(refs[a][...], g, refs[3 * n + a][...], refs[4 * n + a][...])
            outs[a][...] = g
            outs[n + a][...] = d
            outs[2 * n + a][...] = nm
            outs[3 * n + a][...] = nv

        if exchange is not None:
            @pl.when(pl.program_id(0) == steps - 1)
            def _():
                exchange.finish(*exchange.split(ex_refs))

    specs = [pl.BlockSpec((w.shape[0] // steps, w.shape[1]), lambda i: (i, 0)) for w in ws]
    half_specs = [pl.BlockSpec((g.shape[0] // (steps // 2), g.shape[1]), lambda i: (i % (steps // 2), 0)) for g in g_mine]
    shapes = [jax.ShapeDtypeStruct(w.shape, F32) for w in ws]
    res = pl.pallas_call(
        body, name="adamw_big", grid=(steps,), in_specs=specs + half_specs * 2 + specs * 2 + ex_in_specs,
        out_specs=specs * 4 + ex_out_specs, out_shape=shapes * 4 + ex_out_shape, scratch_shapes=ex_scratch,
        compiler_params=_cparams(),
    )(*ws, *g_mine, *g_theirs, *ms, *vs, *ex_in)
    return res[:n], res[n:2 * n], res[2 * n:3 * n], res[3 * n:4 * n], res[4 * n:]


def _adamw_in(w, g_mine, g_theirs, m, v):
    half = D // 2

    def body(w_ref, gm_ref, gt_ref, m_ref, v_ref, g_out, d_out, nm_out, nv_out, g_ref):
        south = lax.axis_index("c") == 0
        g_ref[0:half, :] = jnp.where(south, gm_ref[...], gt_ref[...])
        g_ref[half:D, :] = jnp.where(south, gt_ref[...], gm_ref[...])
        g = g_ref[0:CW, :]
        d, nm, nv = _adamw_math(w_ref[...], g, m_ref[...], v_ref[...])
        g_out[...] = g
        d_out[...] = d
        nm_out[...] = nm
        nv_out[...] = nv

    cols = 2 * LANES
    spec = pl.BlockSpec((CW, cols), lambda i: (0, i))
    half_spec = pl.BlockSpec((half, cols), lambda i: (0, i))
    return pl.pallas_call(
        body, name="adamw_in", grid=(D // cols,), in_specs=[spec, half_spec, half_spec, spec, spec], out_specs=[spec] * 4,
        out_shape=[jax.ShapeDtypeStruct((CW, D), F32)] * 4, scratch_shapes=[pltpu.VMEM((D, cols), F32)],
        compiler_params=_cparams(),
    )(w, g_mine, g_theirs, m, v)


NORM_NAMES = ("pre_mix_norm", "post_mix_norm", "pre_mlp_norm", "post_mlp_norm")
SMALL_NAMES = NORM_NAMES + ("gdn_conv_w", "fox_f_bias", "gdn_dt_bias", "gdn_a_log", "fox_out_norm", "gdn_out_norm")
CONV_COLS = 3 * DGDN // NCHIP


def _small_gather(d_norms, d_conv, sums, d_fox_norm, d_gdn_norm, loss_row):
    n_arrays = 6
    n_remote = n_arrays * (NDEV - 1)

    def copies_of(src, outs, send_sems, recv_sems):
        x, y, c = _place()
        me = 4 * x + 2 * y + c

        def from_me(chip_index):
            cols = pl.ds(pl.multiple_of(chip_index * CONV_COLS, LANES), CONV_COLS)
            return [src[0], src[1].at[:, cols], src[2], src[3], src[4], src[5]]

        local = [pltpu.make_async_copy(s, outs[a].at[me], send_sems.at[n_remote + a]) for a, s in enumerate(from_me(2 * x + y))]
        remote = []
        for k in range(1, NDEV):
            px, py, pc = x ^ ((k >> 2) & 1), y ^ ((k >> 1) & 1), c ^ (k & 1)
            remote += [pltpu.make_async_remote_copy(
                src_ref=s, dst_ref=outs[a].at[me], send_sem=send_sems.at[n_arrays * (k - 1) + a],
                recv_sem=recv_sems.at[n_arrays * (k - 1) + a], device_id=(px, py, pc), device_id_type=MESH)
                for a, s in enumerate(from_me(2 * px + py))]
        return local + remote

    def start(*refs):
        for cp in copies_of(*refs):
            cp.start()

    def finish(*refs):
        for cp in copies_of(*refs):
            cp.wait()

    shapes = [(4, D), (CONV_K, CONV_COLS), (8, LANES), (1, LANES), (1, LANES), (1, LANES)]
    return _Exchange([d_norms, d_conv, sums, d_fox_norm, d_gdn_norm, loss_row],
                     [jax.ShapeDtypeStruct((NDEV,) + s, F32) for s in shapes], n_remote + n_arrays, start, finish)


def _small_adamw(gathered, ws, ms, vs):
    n = len(SMALL_NAMES)
    ng = len(gathered)

    def body(*refs):
        def total(buf):
            acc = buf[0]
            for i in range(1, NDEV):
                acc = acc + buf[i]
            return acc

        t_norms, t_conv, t_sums, t_fn, t_gn, t_loss = [total(r) for r in refs[:ng]]
        w_refs, m_refs, v_refs = refs[ng:ng + n], refs[ng + n:ng + 2 * n], refs[ng + 2 * n:ng + 3 * n]
        outs = refs[ng + 3 * n:]
        outs[4 * n][...] = t_loss
        grads = [t_norms[i:i + 1, :] for i in range(4)] + [
            t_conv, t_sums[0:1, 0:NFH], t_sums[1:2, 0:NGH], t_sums[2:3, 0:NGH], t_fn[:, 0:FHD], t_gn]
        for a in range(n):
            d, nm, nv = _adamw_math(w_refs[a][...], grads[a], m_refs[a][...], v_refs[a][...])
            outs[a][...] = grads[a]
            outs[n + a][...] = d
            outs[2 * n + a][...] = nm
            outs[3 * n + a][...] = nv

    def whole(arr):
        return pl.BlockSpec(arr.shape, lambda i: (0,) * arr.ndim)

    res = pl.pallas_call(
        body, name="small_adamw", grid=(1,), in_specs=[whole(t) for t in gathered] + [whole(w) for w in ws] * 3,
        out_specs=[whole(w) for w in ws] * 4 + [pl.BlockSpec((1, LANES), lambda i: (0, 0))],
        out_shape=[jax.ShapeDtypeStruct(w.shape, F32) for w in ws] * 4 + [jax.ShapeDtypeStruct((1, LANES), F32)],
        compiler_params=_cparams(),
    )(*gathered, *ws, *ms, *vs)
    return res[:n], res[n:2 * n], res[2 * n:3 * n], res[3 * n:4 * n], res[4 * n]


CW = DPROJ // NCHIP
PROJ_RUNS = tuple((part * DFOX + hp * LANES, part * DFOX + (hp + 1) * LANES, (3 * hp + part) * LANES)
                  for hp in range(NPAIR) for part in range(3)) + (
    (1536, 1544, BLK_SMALL * LANES), (1544, 3080, BLK_GDN * LANES), (3080, 3088, BLK_SMALL * LANES + 8),
    (3088, 3600, BLK_GZ * LANES))


def _proj_pieces():
    pieces = []
    for lo, hi, at in PROJ_RUNS:
        while lo < hi:
            j = lo // CW
            end = min(hi, (j + 1) * CW)
            pieces.append((j, lo - j * CW, at, end - lo))
            at, lo = at + end - lo, end
    return pieces


RT = 256


def _to_padded_rows(gathered):
    def body(src_ref, out_ref, blocks_ref, rows_ref):
        blocks_ref[...] = src_ref[...].astype(F32)
        rows_ref[...] = jnp.zeros_like(rows_ref)
        for j, start, at, n in _proj_pieces():
            rows_ref[at:at + n, :] = blocks_ref[j, start:start + n, :]
        out_ref[...] = rows_ref[...].astype(out_ref.dtype)

    return pl.pallas_call(
        body, name="proj_rows_in", grid=(D // RT,), in_specs=[pl.BlockSpec((NCHIP, D, RT), lambda i: (0, 0, i))],
        out_specs=pl.BlockSpec((DPROJ_PAD, RT), lambda i: (0, i)), out_shape=jax.ShapeDtypeStruct((DPROJ_PAD, D), gathered.dtype),
        scratch_shapes=[pltpu.VMEM((NCHIP, D, RT), F32), pltpu.VMEM((DPROJ_PAD, RT), F32)], compiler_params=_cparams(),
    )(gathered)


def _from_padded_rows_pair_sum(w):
    steps = D // RT
    half = D // 2

    def body(src_ref, out_ref, rows_ref, blocks_ref, mine_ref, send_ref, recv_ref, send_sems, recv_sems):
        i = pl.program_id(0)
        x, y, c = _place()

        def share(t):
            return pltpu.make_async_remote_copy(
                src_ref=send_ref.at[t], dst_ref=recv_ref.at[t], send_sem=send_sems.at[t], recv_sem=recv_sems.at[t],
                device_id=(x, y, 1 - c), device_id_type=MESH)

        def rows_of(core):
            return blocks_ref[:, pl.ds(pl.multiple_of(core * half, half), half), :]

        @pl.when(i < steps)
        def _():
            rows_ref[...] = src_ref[...].astype(F32)
            blocks_ref[...] = jnp.zeros_like(blocks_ref)
            for j, start, at, n in _proj_pieces():
                blocks_ref[j, start:start + n, :] = rows_ref[at:at + n, :]
            mine_ref[i % 2] = rows_of(c)
            send_ref[i] = rows_of(1 - c).astype(BF16)
            share(i).start()

        @pl.when(i > 0)
        def _():
            share(i - 1).wait_recv()
            out_ref[...] = (mine_ref[(i - 1) % 2] + recv_ref[i - 1].astype(F32)).astype(BF16)

        @pl.when(i == steps)
        def _():
            for t in range(steps):
                share(t).wait_send()

    tile = (NCHIP, half, RT)
    return pl.pallas_call(
        body, name="proj_rows_out_pair_sum", grid=(steps + 1,),
        in_specs=[pl.BlockSpec((DPROJ_PAD, RT), lambda i: (0, jnp.minimum(i, steps - 1)))],
        out_specs=pl.BlockSpec(tile, lambda i: (0, 0, jnp.maximum(i - 1, 0))),
        out_shape=jax.ShapeDtypeStruct((NCHIP, half, D), BF16),
        scratch_shapes=[pltpu.VMEM((DPROJ_PAD, RT), F32), pltpu.VMEM((NCHIP, D, RT), F32), pltpu.VMEM((2,) + tile, F32),
                        pltpu.VMEM((steps,) + tile, BF16), pltpu.VMEM((steps,) + tile, BF16),
                        pltpu.SemaphoreType.DMA((steps,)), pltpu.SemaphoreType.DMA((steps,))],
        compiler_params=_cparams(),
    )(w)


def _local_step(x, target, first_weights, late_weights, reduce_late, reduce_in, pre_mix_norm, fox_f_bias, fox_out_norm,
                gdn_a_log, gdn_dt_bias, gdn_out_norm, post_mix_norm, pre_mlp_norm, post_mlp_norm):
    bias_vec = jnp.zeros((1, LANES), F32).at[0, 0:NFH].set(fox_f_bias).at[0, LANE_G:LANE_G + NGH].set(gdn_dt_bias)
    alog_vec = jnp.zeros((1, LANES), F32).at[0, LANE_G:LANE_G + NGH].set(gdn_a_log)
    w2 = jnp.concatenate([fox_out_norm, fox_out_norm], axis=1)

    h, first = _pre_norm(x, pre_mix_norm, exchange=first_weights[0])
    win_p, conv_w = first_weights[1](first)
    proj = _matmul(h, win_p, tb=True, tm=2048, tn=768, tk=1024, name="mm_proj", exchange=late_weights[0])
    proj, late_a = proj if late_weights[0] is not None else (proj, [])
    gates = _gates(proj, bias_vec, alog_vec)
    mix, fox_o, lse, late_b = _fox_fwd(proj, gates, w2, exchange=late_weights[1])
    qkv = _gdn_pre(proj, conv_w)
    (u, w, qd, kd, a_intra, gl, t_inv), _ = _gdn_prep(qkv, gates)
    wout, wup3 = late_weights[3](late_a, late_b)
    mix, gdn_raw, states = _gdn_scan(u, w, qd, kd, a_intra, gl, proj, gdn_out_norm, mix)
    def post_mix(acc, xv, w_post, w_pre_mlp):
        x1v = xv + acc * _rms_scale(acc) * w_post
        return acc, x1v, x1v * _rms_scale(x1v) * w_pre_mlp

    mixed, x1, h2 = _matmul(mix, wout, tm=512, tn=D, tk=1024, out_dtypes=(F32, F32, BF16), name="mm_out",
                            extra=(x, post_mix_norm, pre_mlp_norm), epilogue=post_mix)

    def relu2(acc):
        r = jnp.maximum(acc, 0.0)
        return r, r * r

    up_act = _matmul(h2, wup3, b3=True, tm=1024, tn=1024, tk=1024, out_dtypes=(BF16, BF16), epilogue=relu2,
                     name="mm_up", exchange=late_weights[2])
    (up_relu, act), late_c = up_act if late_weights[2] is not None else (up_act, [])
    wdown = late_weights[4](late_c)
    def loss_head(acc, x1v, tv, w):
        err = x1v + acc * _rms_scale(acc) * w - tv
        dx2v = err * (1.0 / D)
        dyv, dwt = _rms_bwd(acc, w, dx2v)
        part = 0.5 * jnp.sum(jnp.mean(err * err, axis=-1, keepdims=True), axis=0, keepdims=True)
        return dx2v, dyv, jnp.sum(dwt, axis=0, keepdims=True), jnp.broadcast_to(part, (1, D))

    dx2, dy, d_post_mlp, loss_wide = _matmul(
        act, wdown, tm=512, tn=D, tk=DFF, out_dtypes=(F32, BF16, F32, F32), extra=(x1, target, post_mlp_norm),
        epilogue=loss_head, n_sums=2, name="mm_down")
    loss_row = loss_wide[:, :LANES]

    dwdown = _matmul(act, dy, ta=True, tm=1024, tn=1024, tk=2048, out_dtypes=(BF16,), name="mm_dwdown")

    def relu2_bwd(acc, r):
        return (acc * 2.0 * r.astype(F32),)

    dup = _matmul(dy, wdown, tb=True, tm=1024, tn=1024, tk=1024, out_dtypes=(BF16,), extra=(up_relu,), epilogue=relu2_bwd,
                  name="mm_dact")
    dwup3 = _matmul(h2, dup, ta=True, tm=1024, tn=1024, tk=2048, out_dtypes=(BF16,), o3=True, name="mm_dwup")
    def mid_bwd(acc, x1v, dx2v, mixedv, w_pre_mlp, w_post):
        dxa, dwm = _rms_bwd(x1v, w_pre_mlp, acc)
        dx1v = dx2v + dxa
        dm, dwp = _rms_bwd(mixedv, w_post, dx1v)
        return dx1v, dm, jnp.sum(dwm, axis=0, keepdims=True), jnp.sum(dwp, axis=0, keepdims=True)

    dx1, dmixed, d_pre_mlp, d_post_mix = _matmul(
        dup, wup3, tb=True, b3=True, tm=512, tn=D, tk=DFF, out_dtypes=(F32, BF16, F32, F32),
        extra=(x1, dx2, mixed, pre_mlp_norm, post_mix_norm), epilogue=mid_bwd, n_sums=2, name="mm_dh2")
    dwout = _matmul(mix, dmixed, ta=True, tm=1024, tn=1024, tk=2048, out_dtypes=(BF16,), name="mm_dwout")
    dmix = _matmul(dmixed, wout, tb=True, tm=2048, tk=1024, name="mm_dmix")

    dfox, delta, d_fox_norm, from_sibling = _fox_norm_bwd(fox_o, dmix, w2, exchange=reduce_late[0](dwout, dwup3, dwdown))
    dproj, dcum_fox, reduced_a = _fox_bwd(proj, dfox, gates, lse, delta, exchange=reduce_late[1](from_sibling))
    (dproj, du, dw, dqd, dkd, da, dgl, d_gdn_norm), reduced_b = _gdn_scan_bwd(
        dmix, gdn_raw, proj, gdn_out_norm, u, w, qd, kd, a_intra, gl, states, dproj, exchange=reduce_late[2]())
    dqkv, dgates_gdn, reduced_c = _gdn_prep_bwd(qkv, gates, t_inv, du, dw, dqd, dkd, da, dgl, exchange=reduce_late[3]())
    reduced_late = (reduced_a, reduced_b, reduced_c)
    dproj, d_conv = _gdn_pre_bwd(proj, conv_w, dqkv, dproj)
    dproj, sums = _gates_bwd(proj, bias_vec, alog_vec, dgates_gdn, dcum_fox, dproj)

    dwin_p = _matmul(dproj, h, ta=True, tm=1280, tn=1024, tk=2048, out_dtypes=(BF16,), name="mm_dwin")
    exchange_in = reduce_in(dwin_p)
    dh = _matmul(dproj, win_p, tm=1024, tk=DPROJ_PAD, name="mm_dh", exchange=exchange_in)
    dh, reduced_in = dh if exchange_in is not None else (dh, [])
    grad_x, d_pre_mix = _pre_norm_bwd(dh, x, pre_mix_norm, dx1)

    d_norms = jnp.concatenate([d_pre_mix, d_post_mix, d_pre_mlp, d_post_mlp], axis=0)
    return grad_x, (d_norms, d_conv, sums, d_fox_norm, d_gdn_norm, loss_row), reduced_late, reduced_in


def kernel(x, pre_mix_norm, w_in, fox_f_bias, fox_out_norm, gdn_conv_w, gdn_a_log, gdn_dt_bias, gdn_out_norm, w_out, post_mix_norm, pre_mlp_norm, w_up, w_down, post_mlp_norm, loss_target, m_pre_mix_norm, m_w_in, m_fox_f_bias, m_fox_out_norm, m_gdn_conv_w, m_gdn_a_log, m_gdn_dt_bias, m_gdn_out_norm, m_w_out, m_post_mix_norm, m_pre_mlp_norm, m_w_up, m_w_down, m_post_mlp_norm, v_pre_mix_norm, v_w_in, v_fox_f_bias, v_fox_out_norm, v_gdn_conv_w, v_gdn_a_log, v_gdn_dt_bias, v_gdn_out_norm, v_w_out, v_post_mix_norm, v_pre_mlp_norm, v_w_up, v_w_down, v_post_mlp_norm):
    weights = dict(pre_mix_norm=pre_mix_norm, w_in=w_in, fox_f_bias=fox_f_bias, fox_out_norm=fox_out_norm, gdn_conv_w=gdn_conv_w,
                   gdn_a_log=gdn_a_log, gdn_dt_bias=gdn_dt_bias, gdn_out_norm=gdn_out_norm, w_out=w_out, post_mix_norm=post_mix_norm,
                   pre_mlp_norm=pre_mlp_norm, w_up=w_up, w_down=w_down, post_mlp_norm=post_mlp_norm)
    m_in = dict(pre_mix_norm=m_pre_mix_norm, w_in=m_w_in, fox_f_bias=m_fox_f_bias, fox_out_norm=m_fox_out_norm, gdn_conv_w=m_gdn_conv_w,
                gdn_a_log=m_gdn_a_log, gdn_dt_bias=m_gdn_dt_bias, gdn_out_norm=m_gdn_out_norm, w_out=m_w_out, post_mix_norm=m_post_mix_norm,
                pre_mlp_norm=m_pre_mlp_norm, w_up=m_w_up, w_down=m_w_down, post_mlp_norm=m_post_mlp_norm)
    v_in = dict(pre_mix_norm=v_pre_mix_norm, w_in=v_w_in, fox_f_bias=v_fox_f_bias, fox_out_norm=v_fox_out_norm, gdn_conv_w=v_gdn_conv_w,
                gdn_a_log=v_gdn_a_log, gdn_dt_bias=v_gdn_dt_bias, gdn_out_norm=v_gdn_out_norm, w_out=v_w_out, post_mix_norm=v_post_mix_norm,
                pre_mlp_norm=v_pre_mlp_norm, w_up=v_w_up, w_down=v_w_down, post_mlp_norm=v_post_mlp_norm)
    order_w = ("pre_mix_norm", "w_in", "fox_f_bias", "fox_out_norm", "gdn_conv_w", "gdn_a_log", "gdn_dt_bias", "gdn_out_norm", "w_out",
               "post_mix_norm", "pre_mlp_norm", "w_up", "w_down", "post_mlp_norm")
    big = ("w_in", "w_out", "w_up", "w_down")

    def row(v):
        return v if v.ndim == 2 else v.reshape(1, -1)

    win_shard = jnp.pad(w_in.T.astype(BF16), ((0, D - CW), (0, 0)))

    def resolve_first(gathered):
        win_g, conv_g = gathered
        return (_to_padded_rows(_with_own(win_g, win_shard)),
                _with_own(conv_g, gdn_conv_w).transpose(1, 0, 2).reshape(CONV_K, 3 * DGDN))

    late_shards = [weights[n].astype(BF16) for n in big[1:]]

    gathered_down = []

    def resolve_out_up(gathered_out, gathered_mlp):
        gathered_down.append(gathered_mlp[1])
        return _with_own(gathered_out[0], late_shards[0]).reshape(D, D), _with_own(gathered_mlp[0], late_shards[1])

    def resolve_down(_):
        return _with_own(gathered_down[0], late_shards[2]).reshape(DFF, D)

    pair_sums, late_blocks = {}, []

    def pair_summed(names, blocks, theirs):
        for n, s in zip(names, _pair_sum(blocks, theirs, "grads_pair_sum_" + names[0])):
            pair_sums[n] = s

    def late_pair_exchange(dwout, dwup3, dwdown):
        late_blocks.extend([dwout.reshape(NCHIP, D // NCHIP, D), dwup3, dwdown.reshape(NCHIP, DFF // NCHIP, D)])
        return _pair_exchange(late_blocks)

    def late_chip_exchange(theirs):
        pair_summed(big[1:], late_blocks, theirs)
        return _chip_exchange([pair_sums["w_up"], pair_sums["w_down"]])

    def reduce_in(dwin_p):
        pair_sums["w_in"] = _from_padded_rows_pair_sum(dwin_p)
        return _chip_exchange([pair_sums["w_in"]])

    grad_x, small, received_late, received_in = _local_step(
        x[0], loss_target[0], (_allgather_exchange([win_shard], whole=[gdn_conv_w]), resolve_first),
        (_allgather_exchange(late_shards[:1]), _allgather_exchange(late_shards[1:]), None, resolve_out_up, resolve_down),
        (late_pair_exchange, late_chip_exchange, lambda: None, lambda: _chip_exchange([pair_sums["w_out"]])),
        reduce_in, row(pre_mix_norm), fox_f_bias, row(fox_out_norm), gdn_a_log, gdn_dt_bias,
        row(gdn_out_norm), row(post_mix_norm), row(pre_mlp_norm), row(post_mlp_norm))
    received_mlp, _, received_out = received_late

    g_mine, g_theirs, small_gathered = _chip_sum(
        [pair_sums[n] for n in big], list(received_in[:1]) + list(received_out[:1]) + list(received_mlp[:2]),
        exchange=_small_gather(*small))

    g_big, d_big, nm_big, nv_big, _ = _adamw_big(
        [weights[n] for n in big[1:]], g_mine[1:], g_theirs[1:], [m_in[n] for n in big[1:]], [v_in[n] for n in big[1:]])
    in_t = _adamw_in(w_in.T, g_mine[0], g_theirs[0], m_w_in.T, v_w_in.T)
    g_small, d_small, nm_small, nv_small, loss_total = _small_adamw(
        small_gathered, [row(weights[n]) for n in SMALL_NAMES], [row(m_in[n]) for n in SMALL_NAMES],
        [row(v_in[n]) for n in SMALL_NAMES])

    grads, delta, new_m, new_v = {}, {}, {}, {}
    grads["w_in"], delta["w_in"], new_m["w_in"], new_v["w_in"] = [t.T for t in in_t]
    for i, n in enumerate(big[1:]):
        grads[n], delta[n], new_m[n], new_v[n] = g_big[i], d_big[i], nm_big[i], nv_big[i]
    for i, n in enumerate(SMALL_NAMES):
        shape = weights[n].shape
        grads[n], delta[n], new_m[n], new_v[n] = (g_small[i].reshape(shape), d_small[i].reshape(shape),
                                                  nm_small[i].reshape(shape), nv_small[i].reshape(shape))
    return (loss_total[0, 0], grad_x[None], *[grads[n] for n in order_w], *[delta[n] for n in order_w], *[new_m[n] for n in order_w],
            *[new_v[n] for n in order_w])
```

```python
import jax
import jax.numpy as jnp
from jax import lax
from jax.experimental import pallas as pl
from jax.experimental.pallas import tpu as pltpu

F32 = jnp.float32
BF16 = jnp.bfloat16
MESH = pl.DeviceIdType.MESH

S = 2048
D = 1024
NFH, FHD = 8, 64
NPAIR = NFH // 2
NGH, GHD = 4, 128
DFOX = NFH * FHD
DGDN = NGH * GHD
CHUNK = 64
NCH = S // CHUNK
CONV_K = 4
DFF = 4 * D
EPS = 1e-6
DPROJ = 3600
LANES = 128
DPROJ_PAD = 3840
BLK_GDN = 12
BLK_GZ = 24
BLK_SMALL = 28
NCHIP = 4
NDEV = 8
VMEM_LIMIT = 56 * 1024 * 1024

ADAM_LR = 0.001
ADAM_B1 = 0.9
ADAM_B2 = 0.999
ADAM_EPS = 1e-08
ADAM_WD = 0.01
ADAM_STEP = 10


def _cparams(**kw):
    return pltpu.CompilerParams(vmem_limit_bytes=VMEM_LIMIT, **kw)


def _dn(ca, cb):
    return (((ca,), (cb,)), ((), ()))


def _dot(a, b, ca=1, cb=0):
    return lax.dot_general(a.astype(BF16), b.astype(BF16), _dn(ca, cb), preferred_element_type=F32)


def _hdot(a, b, ca=1, cb=0):
    return lax.dot_general(a.astype(F32), b.astype(F32), _dn(ca, cb), precision=lax.Precision.HIGHEST,
                           preferred_element_type=F32)


def _dot3(a, b, ca=1, cb=0):
    a_hi, b_hi = a.astype(BF16), b.astype(BF16)
    a_lo, b_lo = (a - a_hi.astype(F32)).astype(BF16), (b - b_hi.astype(F32)).astype(BF16)
    dn = _dn(ca, cb)
    return (lax.dot_general(a_hi, b_hi, dn, preferred_element_type=F32)
            + (lax.dot_general(a_hi, b_lo, dn, preferred_element_type=F32)
               + lax.dot_general(a_lo, b_hi, dn, preferred_element_type=F32)))


@jax.custom_vjp
def _mm_nn(a, b):
    return _dot(a, b, 1, 0)


def _mm_nn_fwd(a, b):
    return _dot(a, b, 1, 0), (a, b)


def _mm_nn_bwd(res, g):
    a, b = res
    return _dot(g, b, 1, 1), _dot(a, g, 0, 0)


_mm_nn.defvjp(_mm_nn_fwd, _mm_nn_bwd)


@jax.custom_vjp
def _mm_nt(a, b):
    return _dot(a, b, 1, 1)


def _mm_nt_fwd(a, b):
    return _dot(a, b, 1, 1), (a, b)


def _mm_nt_bwd(res, g):
    a, b = res
    return _dot(g, b, 1, 0), _dot(g, a, 0, 0)


_mm_nt.defvjp(_mm_nt_fwd, _mm_nt_bwd)


@jax.custom_vjp
def _saved_inverse(m, t_inv):
    del m
    return t_inv


def _saved_inverse_fwd(m, t_inv):
    del m
    return t_inv, t_inv


def _saved_inverse_bwd(t_inv, g):
    return -_dot3(_dot3(t_inv, g, 0, 0), t_inv, 1, 1), jnp.zeros_like(t_inv)


_saved_inverse.defvjp(_saved_inverse_fwd, _saved_inverse_bwd)


def _sigmoid(z):
    return 1.0 / (1.0 + jnp.exp(-z))


def _softplus(z):
    return jnp.maximum(z, 0.0) + jnp.log(1.0 + jnp.exp(-jnp.abs(z)))


def _silu(z):
    return z * _sigmoid(z)


def _rms_scale(x):
    return lax.rsqrt(jnp.mean(x * x, axis=-1, keepdims=True) + EPS)


def _rms_bwd(x, w, g):
    r = _rms_scale(x)
    gw = g * w
    dx = r * gw - x * (r * r * r) * jnp.mean(gw * x, axis=-1, keepdims=True)
    return dx, g * x * r


def _matmul(a, b, *, name, ta=False, tb=False, tm=512, tn=512, tk=512, out_dtypes=(F32,), b3=False, o3=False,
            extra=(), epilogue=None, exchange=None, n_sums=0):
    m, k = (a.shape[1], a.shape[0]) if ta else a.shape
    if b3:
        n = b.shape[1] if tb else b.shape[0] * b.shape[2]
        kb = b.shape[0] * b.shape[2] if tb else b.shape[1]
    else:
        n, kb = (b.shape[0], b.shape[1]) if tb else (b.shape[1], b.shape[0])
    assert kb == k, (name, kb, k)
    tm, tn, tk = min(tm, m), min(tn, n), min(tk, k)
    assert m % tm == 0 and n % tn == 0 and k % tk == 0, (name, m, n, k, tm, tn, tk)
    nk = k // tk
    whole_k_blocks = b3 and tb and not ta and nk == 1 and b.shape[0] > 1
    n_extra = len(extra)
    n_out = len(out_dtypes)
    grid = (m // tm, n // tn, nk)
    ex_in, ex_in_specs, ex_out_specs, ex_out_shape, ex_scratch = _hosted(exchange)

    def body(*refs):
        a_ref, b_ref = refs[0], refs[1]
        extra_refs = refs[2:2 + n_extra]
        first_out = 2 + n_extra + len(ex_in)
        out_refs = refs[first_out:first_out + n_out]
        ex_refs = refs[2 + n_extra:first_out] + refs[first_out + n_out:first_out + n_out + len(ex_out_shape)] + refs[-2:]
        step = [pl.program_id(d) for d in range(3)]

        if exchange is not None:
            @pl.when((step[0] == 0) & (step[1] == 0) & (step[2] == 0))
            def _():
                exchange.start(*exchange.split(ex_refs))

        def finish(acc):
            outs = (acc,) if epilogue is None else epilogue(acc, *[r[...] for r in extra_refs])
            for o_ref, val in zip(out_refs[:n_out - n_sums], outs):
                o_ref[...] = val.astype(o_ref.dtype)
            for o_ref, val in zip(out_refs[n_out - n_sums:], outs[n_out - n_sums:]):
                @pl.when(step[0] == 0)
                def _(o_ref=o_ref, val=val):
                    o_ref[...] = val

                @pl.when(step[0] > 0)
                def _(o_ref=o_ref, val=val):
                    o_ref[...] += val

        if whole_k_blocks:
            width = b.shape[2]
            part = _dot(a_ref[:, 0:width], b_ref[0], 1, 1)
            for blk in range(1, b.shape[0]):
                part = part + _dot(a_ref[:, blk * width:(blk + 1) * width], b_ref[blk], 1, 1)
        else:
            part = _dot(a_ref[...], b_ref[...], 0 if ta else 1, 1 if tb else 0)
        if nk == 1:
            finish(part)
        else:
            acc_ref = refs[first_out + n_out + len(ex_out_shape)]

            @pl.when(step[2] == 0)
            def _():
                acc_ref[...] = part

            @pl.when(step[2] > 0)
            def _():
                acc_ref[...] += part

            @pl.when(step[2] == nk - 1)
            def _():
                finish(acc_ref[...])

        if exchange is not None:
            flat = (step[0] * grid[1] + step[1]) * nk + step[2]
            total = grid[0] * grid[1] * nk

            @pl.when(flat == total // 2)
            def _():
                exchange.middle(*exchange.split(ex_refs))

            @pl.when(flat == total - 1)
            def _():
                exchange.rest(*exchange.split(ex_refs))

    a_spec = pl.BlockSpec((tk, tm), lambda i, j, kk: (kk, i)) if ta else pl.BlockSpec((tm, tk), lambda i, j, kk: (i, kk))
    if whole_k_blocks:
        b_spec = pl.BlockSpec((b.shape[0], tn, b.shape[2]), lambda i, j, kk: (0, j, 0))
    elif b3 and tb:
        assert b.shape[2] == tk
        b_spec = pl.BlockSpec((None, tn, tk), lambda i, j, kk: (kk, j, 0))
    elif b3:
        assert b.shape[2] == tn
        b_spec = pl.BlockSpec((None, tk, tn), lambda i, j, kk: (j, kk, 0))
    elif tb:
        b_spec = pl.BlockSpec((tn, tk), lambda i, j, kk: (j, kk))
    else:
        b_spec = pl.BlockSpec((tk, tn), lambda i, j, kk: (kk, j))
    tile = pl.BlockSpec((tm, tn), lambda i, j, kk: (i, j))
    out_specs = [tile] * n_out
    out_shape = [jax.ShapeDtypeStruct((m, n), dt) for dt in out_dtypes]
    if o3:
        out_specs[0] = pl.BlockSpec((None, tm, tn), lambda i, j, kk: (j, i, 0))
        out_shape[0] = jax.ShapeDtypeStruct((n // tn, m, tn), out_dtypes[0])
    assert n_sums == 0 or tn == n
    for r in range(n_out - n_sums, n_out):
        out_specs[r] = pl.BlockSpec((1, tn), lambda i, j, kk: (0, 0))
        out_shape[r] = jax.ShapeDtypeStruct((1, n), out_dtypes[r])
    res = pl.pallas_call(
        body, name=name, grid=grid,
        in_specs=[a_spec, b_spec] + [tile if e.shape[0] == m else pl.BlockSpec((1, tn), lambda i, j, kk: (0, j)) for e in extra]
        + ex_in_specs, out_specs=out_specs + ex_out_specs,
        out_shape=out_shape + ex_out_shape,
        scratch_shapes=([pltpu.VMEM((tm, tn), F32)] if nk > 1 else []) + ex_scratch,
        compiler_params=_cparams(),
    )(a, b, *extra, *ex_in)
    if exchange is not None:
        return (res[0] if n_out == 1 else res[:n_out]), res[n_out:]
    return res[0] if n_out == 1 else res


TR = 512


def _row_spec(cols):
    return pl.BlockSpec((TR, cols), lambda i: (i, 0))


def _vec_spec(cols):
    return pl.BlockSpec((1, cols), lambda i: (0, 0))


def _pre_norm(x, w, exchange=None):
    ex_in, ex_in_specs, ex_out_specs, ex_out_shape, ex_scratch = _hosted(exchange)

    def body(*refs):
        x_ref, w_ref, h_ref = refs[0], refs[1], refs[2 + len(ex_in)]
        ex_refs = refs[2:2 + len(ex_in)] + refs[3 + len(ex_in):]
        if exchange is not None:
            @pl.when(pl.program_id(0) == 0)
            def _():
                exchange.start(*exchange.split(ex_refs))

        xv = x_ref[...]
        h_ref[...] = (xv * _rms_scale(xv) * w_ref[...]).astype(BF16)

        if exchange is not None:
            @pl.when(pl.program_id(0) == S // TR - 1)
            def _():
                exchange.finish(*exchange.split(ex_refs))

    res = pl.pallas_call(
        body, name="pre_norm", grid=(S // TR,), in_specs=[_row_spec(D), _vec_spec(D)] + ex_in_specs,
        out_specs=[_row_spec(D)] + ex_out_specs, out_shape=[jax.ShapeDtypeStruct((S, D), BF16)] + ex_out_shape,
        scratch_shapes=ex_scratch, compiler_params=_cparams(),
    )(x, w, *ex_in)
    return res[0], res[1:]


def _pre_norm_bwd(dh, x, w, dx1):
    def body(dh_ref, x_ref, w_ref, dx1_ref, dx_ref, dw_ref):
        i = pl.program_id(0)
        dxa, dwt = _rms_bwd(x_ref[...], w_ref[...], dh_ref[...])
        dx_ref[...] = dx1_ref[...] + dxa

        @pl.when(i == 0)
        def _():
            dw_ref[...] = jnp.zeros_like(dw_ref)

        dw_ref[...] += jnp.sum(dwt, axis=0, keepdims=True)

    return pl.pallas_call(
        body, name="pre_norm_bwd", grid=(S // TR,),
        in_specs=[_row_spec(D), _row_spec(D), _vec_spec(D), _row_spec(D)], out_specs=[_row_spec(D), _vec_spec(D)],
        out_shape=[jax.ShapeDtypeStruct((S, D), F32), jax.ShapeDtypeStruct((1, D), F32)], compiler_params=_cparams(),
    )(dh, x, w, dx1)


BQ = 512
NQ = S // BQ
LANE_BETA, LANE_G = 8, 12


def _gate_lanes(shape):
    lane = lax.broadcasted_iota(jnp.int32, shape, 1)
    return lane < LANE_BETA, (lane >= LANE_BETA) & (lane < LANE_G), (lane >= LANE_G) & (lane < LANE_G + NGH)


def _gates(proj, bias_vec, alog_vec):
    def body(s_ref, b_ref, a_ref, o_ref, carry_ref):
        i = pl.program_id(0)

        @pl.when(i == 0)
        def _():
            carry_ref[...] = jnp.zeros_like(carry_ref)

        z = s_ref[...] + b_ref[...]
        tail = jnp.log(1.0 + jnp.exp(-jnp.abs(z)))
        sp = jnp.maximum(z, 0.0) + tail
        lf = jnp.minimum(z, 0.0) - tail
        r = lax.broadcasted_iota(jnp.int32, (BQ, BQ), 0)
        c = lax.broadcasted_iota(jnp.int32, (BQ, BQ), 1)
        tri = (c <= r).astype(F32)
        cum = _hdot(tri, lf) + carry_ref[...]
        carry_ref[...] = cum[BQ - 1:BQ, :]
        is_fox, is_beta, is_g = _gate_lanes(z.shape)
        o_ref[...] = jnp.where(is_fox, cum, jnp.where(is_beta, _sigmoid(z), jnp.where(is_g, -jnp.exp(a_ref[...]) * sp, 0.0)))

    return pl.pallas_call(
        body, name="gates", grid=(NQ,),
        in_specs=[pl.BlockSpec((BQ, LANES), lambda i: (i, BLK_SMALL)), _vec_spec(LANES), _vec_spec(LANES)],
        out_specs=pl.BlockSpec((BQ, LANES), lambda i: (i, 0)), out_shape=jax.ShapeDtypeStruct((S, LANES), F32),
        scratch_shapes=[pltpu.VMEM((1, LANES), F32)], compiler_params=_cparams(),
    )(proj, bias_vec, alog_vec)


def _gates_bwd(proj, bias_vec, alog_vec, dgates_gdn, dcum_fox, dproj):
    def body(s_ref, b_ref, a_ref, dg_ref, dc_ref, dproj_in, dproj_ref, red_ref, carry_ref):
        del dproj_in
        i = pl.program_id(0)

        @pl.when(i == 0)
        def _():
            carry_ref[...] = jnp.zeros_like(carry_ref)
            red_ref[...] = jnp.zeros_like(red_ref)

        z = s_ref[...] + b_ref[...]
        dg = dg_ref[...] + dc_ref[...]
        r = lax.broadcasted_iota(jnp.int32, (BQ, BQ), 0)
        c = lax.broadcasted_iota(jnp.int32, (BQ, BQ), 1)
        upper = (c >= r).astype(F32)
        dlf = _hdot(upper, dg) + carry_ref[...]
        carry_ref[...] = dlf[0:1, :]
        sig = _sigmoid(z)
        g_scale = -jnp.exp(a_ref[...])
        is_fox, is_beta, is_g = _gate_lanes(z.shape)
        ds = jnp.where(is_fox, dlf * (1.0 - sig), jnp.where(is_beta, dg * sig * (1.0 - sig), jnp.where(is_g, dg * g_scale * sig, 0.0)))
        dproj_ref[:, 0:LANES] = ds.astype(BF16)
        dproj_ref[:, LANES:2 * LANES] = jnp.zeros((BQ, LANES), BF16)
        dalog = jnp.where(is_g, dg * g_scale * _softplus(z), 0.0)
        sums = jnp.sum(ds, axis=0, keepdims=True)
        red_ref[0:1, :] += jnp.where(is_fox[0:1], sums, 0.0)
        red_ref[1:2, :] += pltpu.roll(jnp.where(is_g[0:1], sums, 0.0), LANES - LANE_G, 1)
        red_ref[2:3, :] += pltpu.roll(jnp.sum(dalog, axis=0, keepdims=True), LANES - LANE_G, 1)

    blk = pl.BlockSpec((BQ, LANES), lambda i: (NQ - 1 - i, 0))
    return pl.pallas_call(
        body, name="gates_bwd", grid=(NQ,),
        in_specs=[pl.BlockSpec((BQ, LANES), lambda i: (NQ - 1 - i, BLK_SMALL)), _vec_spec(LANES), _vec_spec(LANES), blk, blk,
                  pl.BlockSpec(memory_space=pl.ANY)],
        out_specs=[pl.BlockSpec((BQ, 2 * LANES), lambda i: (NQ - 1 - i, BLK_SMALL // 2)), pl.BlockSpec((8, LANES), lambda i: (0, 0))],
        out_shape=[jax.ShapeDtypeStruct((S, DPROJ_PAD), BF16), jax.ShapeDtypeStruct((8, LANES), F32)],
        input_output_aliases={5: 0},
        scratch_shapes=[pltpu.VMEM((1, LANES), F32)], compiler_params=_cparams(),
    )(proj, bias_vec, alog_vec, dgates_gdn, dcum_fox, dproj)


FOX_SCALE = FHD ** -0.5
FOX_PAIRS = 2
FOX_PAIRS_BWD = 2


def _head_mask(e):
    lane = lax.broadcasted_iota(jnp.int32, (1, LANES), 1)
    return (lane >= e * FHD) & (lane < (e + 1) * FHD)


def _lane_col(vals, index):
    lane = lax.broadcasted_iota(jnp.int32, vals.shape, 1)
    return jnp.sum(jnp.where(lane == index, vals, 0.0), axis=1, keepdims=True)


def _sublane_row(vals, index):
    row = lax.broadcasted_iota(jnp.int32, vals.shape, 0)
    return jnp.sum(jnp.where(row == index, vals, 0.0), axis=0, keepdims=True)


def _pair_cols(c0, c1):
    lane = lax.broadcasted_iota(jnp.int32, (c0.shape[0], 2), 1)
    return jnp.where(lane == 0, c0, c1)


def _split3(x):
    hi = x.astype(BF16).astype(F32)
    rest = x - hi
    mid = rest.astype(BF16).astype(F32)
    return hi, mid, (rest - mid).astype(BF16).astype(F32)


def _fox_operand(vals, e, cum, is_query):
    lane = lax.broadcasted_iota(jnp.int32, (1, LANES), 1)
    base = (1 - e) * FHD
    parts = _split3(cum)
    own = jnp.where(_head_mask(e), vals * FOX_SCALE if is_query else vals, 0.0)
    cum_at, ones_at = (base, base + 3) if is_query else (base + 3, base)
    sign = 1.0 if is_query else -1.0
    out = own + jnp.where((lane >= ones_at) & (lane < ones_at + 3), 1.0, 0.0)
    for i, part in enumerate(parts):
        out = out + jnp.where(lane == cum_at + i, sign * part, 0.0)
    return out.astype(BF16)


def _causal_block():
    return lax.broadcasted_iota(jnp.int32, (BQ, BQ), 1) <= lax.broadcasted_iota(jnp.int32, (BQ, BQ), 0)


def _head_rms(o, masks):
    o2 = o * o
    r = [lax.rsqrt(jnp.sum(jnp.where(mk, o2, 0.0), axis=1, keepdims=True) * (1.0 / FHD) + EPS) for mk in masks]
    return jnp.where(masks[0], r[0], r[1])


def _hosted(exchange):
    if exchange is None:
        return [], [], [], [], []
    return (exchange.inputs, [HBM] * len(exchange.inputs), [HBM] * len(exchange.out_shape), exchange.out_shape,
            exchange.sem_shapes())


def _fox_fwd(proj, gates, w2, exchange=None):
    ex_in, ex_in_specs, ex_out_specs, ex_out_shape, ex_scratch = _hosted(exchange)

    n_in = 3 * FOX_PAIRS + 2
    heads = [(pp, e) for pp in range(FOX_PAIRS) for e in range(2)]

    def body(*refs):
        qkv_refs, g_ref, w_ref = refs[:3 * FOX_PAIRS], refs[3 * FOX_PAIRS], refs[3 * FOX_PAIRS + 1]
        mix_ref, o_ref, lse_ref = refs[n_in + len(ex_in):n_in + 3 + len(ex_in)]
        ka_ref, vb_ref = refs[n_in + 3 + len(ex_in) + len(ex_out_shape):n_in + 5 + len(ex_in) + len(ex_out_shape)]
        ex_refs = refs[n_in:n_in + len(ex_in)] + refs[n_in + 3 + len(ex_in):n_in + 3 + len(ex_in) + len(ex_out_shape)] + refs[-2:]
        grp, qi = pl.program_id(0), pl.program_id(1)

        def head_index(pp, e):
            return 2 * (FOX_PAIRS * grp + pp) + e

        if exchange is not None:
            @pl.when((grp == 0) & (qi == 0))
            def _():
                exchange.start(*exchange.split(ex_refs))

        @pl.when(qi == 0)
        def _():
            gt = g_ref[...]
            for pp in range(FOX_PAIRS):
                kv = qkv_refs[3 * pp + 1][...]
                for e in range(2):
                    ka_ref[2 * pp + e] = _fox_operand(kv, e, _lane_col(gt, head_index(pp, e)), False)
                vb_ref[pp] = qkv_refs[3 * pp + 2][...].astype(BF16)

        masks = [_head_mask(0), _head_mask(1)]
        gt = g_ref[pl.ds(pl.multiple_of(qi * BQ, BQ), BQ), :]
        qs = [_fox_operand(qkv_refs[3 * pp][...], e, _lane_col(gt, head_index(pp, e)), True) for pp, e in heads]
        n = range(len(heads))

        def block(kj, carry, diagonal):
            rows = pl.ds(pl.multiple_of(kj * BQ, BQ), BQ)
            s = [_dot(qs[i], ka_ref[i, rows, :], 1, 1) for i in n]
            if diagonal:
                s = [jnp.where(_causal_block(), s[i], -jnp.inf) for i in n]
            m_new = [jnp.maximum(carry[i][0], jnp.max(s[i], axis=-1, keepdims=True)) for i in n]
            p = [jnp.exp(s[i] - m_new[i]) for i in n]
            alpha = [jnp.exp(carry[i][0] - m_new[i]) for i in n]
            l_new = [alpha[i] * carry[i][1] + jnp.sum(p[i], axis=-1, keepdims=True) for i in n]
            pv = [_dot(p[i], vb_ref[heads[i][0], rows, :]) for i in n]
            return tuple((m_new[i], l_new[i], alpha[i] * carry[i][2] + pv[i]) for i in n)

        one = (jnp.full((BQ, 1), -jnp.inf, F32), jnp.zeros((BQ, 1), F32), jnp.zeros((BQ, LANES), F32))
        below = lax.fori_loop(0, qi, lambda kj, carry: block(kj, carry, False), (one,) * len(heads))
        done = block(qi, below, True)
        for pp in range(FOX_PAIRS):
            (m0, l0, a0), (m1, l1, a1) = done[2 * pp], done[2 * pp + 1]
            o = jnp.where(masks[0], a0 / l0, a1 / l1)
            cols = slice(pp * LANES, (pp + 1) * LANES)
            o_ref[:, cols] = o
            mix_ref[:, cols] = (o * _head_rms(o, masks) * w_ref[...]).astype(BF16)
            lse_ref[pp] = _pair_cols(m0 + jnp.log(l0), m1 + jnp.log(l1))

        if exchange is not None:
            @pl.when((grp == NPAIR // FOX_PAIRS // 2) & (qi == 0))
            def _():
                exchange.middle(*exchange.split(ex_refs))

            @pl.when((grp == NPAIR // FOX_PAIRS - 1) & (qi == NQ - 1))
            def _():
                exchange.rest(*exchange.split(ex_refs))

    qkv_specs = []
    for pp in range(FOX_PAIRS):
        qkv_specs.append(pl.BlockSpec((BQ, LANES), lambda g, i, pp=pp: (i, 3 * (FOX_PAIRS * g + pp))))
        qkv_specs.append(pl.BlockSpec((S, LANES), lambda g, i, pp=pp: (0, 3 * (FOX_PAIRS * g + pp) + 1)))
        qkv_specs.append(pl.BlockSpec((S, LANES), lambda g, i, pp=pp: (0, 3 * (FOX_PAIRS * g + pp) + 2)))
    blk = pl.BlockSpec((BQ, FOX_PAIRS * LANES), lambda g, i: (i, g))
    res = pl.pallas_call(
        body, name="fox_fwd", grid=(NPAIR // FOX_PAIRS, NQ),
        in_specs=qkv_specs + [pl.BlockSpec((S, LANES), lambda g, i: (0, 0)), pl.BlockSpec((1, LANES), lambda g, i: (0, 0))]
        + ex_in_specs,
        out_specs=[blk, blk, pl.BlockSpec((FOX_PAIRS, BQ, 2), lambda g, i: (g, i, 0))] + ex_out_specs,
        out_shape=[jax.ShapeDtypeStruct((S, D), BF16), jax.ShapeDtypeStruct((S, DFOX), F32),
                   jax.ShapeDtypeStruct((NPAIR, S, 2), F32)] + ex_out_shape,
        scratch_shapes=[pltpu.VMEM((2 * FOX_PAIRS, S, LANES), BF16), pltpu.VMEM((FOX_PAIRS, S, LANES), BF16)] + ex_scratch,
        compiler_params=_cparams(),
    )(*([proj] * (3 * FOX_PAIRS)), gates, w2, *ex_in)
    return res[0], res[1], res[2], res[3:]


def _fox_norm_bwd(o, dmix, w2, exchange=None):
    ex_in, ex_in_specs, ex_out_specs, ex_out_shape, ex_scratch = _hosted(exchange)

    def body(*refs):
        o_ref, g_ref, w_ref = refs[:3]
        do_ref, dl_ref, dw_ref = refs[3 + len(ex_in):6 + len(ex_in)]
        ex_refs = refs[3:3 + len(ex_in)] + refs[6 + len(ex_in):]
        hp, qi = pl.program_id(0), pl.program_id(1)

        if exchange is not None:
            @pl.when((hp == 0) & (qi == 0))
            def _():
                exchange.start(*exchange.split(ex_refs))

        masks = [_head_mask(0), _head_mask(1)]
        ov = o_ref[...]
        g = g_ref[...]
        r = _head_rms(ov, masks)
        gw = g * w_ref[...]
        gwo = gw * ov
        mean = [jnp.sum(jnp.where(mk, gwo, 0.0), axis=1, keepdims=True) * (1.0 / FHD) for mk in masks]
        do = r * gw - ov * (r * r * r) * jnp.where(masks[0], mean[0], mean[1])
        do_ref[...] = do.astype(BF16)
        doo = do * ov
        dl_ref[...] = _pair_cols(*[jnp.sum(jnp.where(mk, doo, 0.0), axis=1, keepdims=True) for mk in masks])

        @pl.when((hp == 0) & (qi == 0))
        def _():
            dw_ref[...] = jnp.zeros_like(dw_ref)

        dw_ref[...] += jnp.sum(g * ov * r, axis=0, keepdims=True)

        @pl.when((hp == NPAIR - 1) & (qi == NQ - 1))
        def _():
            dw = dw_ref[...]
            dw_ref[...] = dw + pltpu.roll(dw, FHD, 1)
            if exchange is not None:
                exchange.finish(*exchange.split(ex_refs))

    blk = pl.BlockSpec((BQ, LANES), lambda hp, i: (i, hp))
    vec = pl.BlockSpec((1, LANES), lambda hp, i: (0, 0))
    res = pl.pallas_call(
        body, name="fox_norm_bwd", grid=(NPAIR, NQ), in_specs=[blk, blk, vec] + ex_in_specs,
        out_specs=[blk, pl.BlockSpec((None, BQ, 2), lambda hp, i: (hp, i, 0)), vec] + ex_out_specs,
        out_shape=[jax.ShapeDtypeStruct((S, DFOX), BF16), jax.ShapeDtypeStruct((NPAIR, S, 2), F32),
                   jax.ShapeDtypeStruct((1, LANES), F32)] + ex_out_shape,
        scratch_shapes=ex_scratch, compiler_params=_cparams(),
    )(o, dmix, w2, *ex_in)
    return res[0], res[1], res[2], res[3:]


def _fox_bwd(proj, do, gates, lse, delta, exchange=None):
    ex_in, ex_in_specs, ex_out_specs, ex_out_shape, ex_scratch = _hosted(exchange)

    pg = FOX_PAIRS_BWD
    n_in = 3 * pg + 4
    heads = [(pp, e) for pp in range(pg) for e in range(2)]

    def body(*refs):
        qkv_refs = refs[:3 * pg]
        do_ref, g_ref, lse_ref, dl_ref = refs[3 * pg:n_in]
        dproj_ref, dc_ref = refs[n_in + len(ex_in):n_in + 2 + len(ex_in)]
        qa_ref, dq_ref = refs[n_in + 2 + len(ex_in) + len(ex_out_shape):n_in + 4 + len(ex_in) + len(ex_out_shape)]
        ex_refs = refs[n_in:n_in + len(ex_in)] + refs[n_in + 2 + len(ex_in):n_in + 2 + len(ex_in) + len(ex_out_shape)] + refs[-2:]
        grp, kj = pl.program_id(0), pl.program_id(1)

        def head_index(pp, e):
            return 2 * (pg * grp + pp) + e

        if exchange is not None:
            @pl.when((grp == 0) & (kj == 0))
            def _():
                exchange.start(*exchange.split(ex_refs))

        @pl.when(kj == 0)
        def _():
            gt = g_ref[...]
            for pp in range(pg):
                qv = qkv_refs[3 * pp][...]
                for e in range(2):
                    qa_ref[2 * pp + e] = _fox_operand(qv, e, _lane_col(gt, head_index(pp, e)), True)
            dq_ref[...] = jnp.zeros_like(dq_ref)

        @pl.when((grp == 0) & (kj == 0))
        def _():
            dc_ref[...] = jnp.zeros_like(dc_ref)

        masks = [_head_mask(0), _head_mask(1)]
        krows = pl.ds(pl.multiple_of(kj * BQ, BQ), BQ)
        gk = g_ref[krows, :]
        kas = [_fox_operand(qkv_refs[3 * pp + 1][...], e, _lane_col(gk, head_index(pp, e)), False) for pp, e in heads]
        vbs = [qkv_refs[3 * pp + 2][...].astype(BF16) for pp in range(pg)]
        lane = lax.broadcasted_iota(jnp.int32, (BQ, LANES), 1)
        n = range(len(heads))

        def block(qi, carry, diagonal):
            dks, dvs, css = carry
            rows = pl.ds(pl.multiple_of(qi * BQ, BQ), BQ)
            qa = [qa_ref[i, rows, :] for i in n]
            s = [_dot(qa[i], kas[i], 1, 1) for i in n]
            if diagonal:
                s = [jnp.where(_causal_block(), s[i], -jnp.inf) for i in n]
            dov = [do_ref[rows, pp * LANES:(pp + 1) * LANES] for pp in range(pg)]
            doe = [jnp.where(masks[e], dov[pp], jnp.zeros_like(dov[pp])) for pp, e in heads]
            lse2 = [lse_ref[pp, rows, :] for pp in range(pg)]
            dl2 = [dl_ref[pp, rows, :] for pp in range(pg)]
            p = [jnp.exp(s[i] - _lane_col(lse2[heads[i][0]], heads[i][1])) for i in n]
            dp = [_dot(doe[i], vbs[heads[i][0]], 1, 1) for i in n]
            ds = [p[i] * (dp[i] - _lane_col(dl2[heads[i][0]], heads[i][1])) for i in n]
            dv_part = [_dot(p[i], doe[i], 0, 0) for i in n]
            dk_part = [_dot(ds[i], jnp.where(masks[heads[i][1]], qa[i], jnp.zeros_like(qa[i])), 0, 0) for i in n]
            dq_part = [jnp.where(masks[heads[i][1]], _dot(ds[i], kas[i]), 0.0) for i in n]
            css = tuple(css[i] + jnp.sum(ds[i], axis=0, keepdims=True) for i in n)
            dc = jnp.zeros((BQ, LANES), F32)
            for i in n:
                dc = dc + jnp.where(lane == head_index(*heads[i]), jnp.sum(ds[i], axis=1, keepdims=True), 0.0)
            for pp in range(pg):
                dq_ref[pp, rows, :] += (dq_part[2 * pp] + dq_part[2 * pp + 1]) * FOX_SCALE
            dc_ref[rows, :] += dc
            dks = tuple(dks[pp] + dk_part[2 * pp] + dk_part[2 * pp + 1] for pp in range(pg))
            dvs = tuple(dvs[pp] + dv_part[2 * pp] + dv_part[2 * pp + 1] for pp in range(pg))
            return dks, dvs, css

        zero = jnp.zeros((BQ, LANES), F32)
        first = block(kj, ((zero,) * pg, (zero,) * pg, (jnp.zeros((1, BQ), F32),) * len(heads)), True)
        dks, dvs, css = lax.fori_loop(kj + 1, NQ, lambda qi, carry: block(qi, carry, False), first)
        r = lax.broadcasted_iota(jnp.int32, (BQ, BQ), 0)
        c = lax.broadcasted_iota(jnp.int32, (BQ, BQ), 1)
        dcol = jnp.zeros((BQ, LANES), F32)
        for i in n:
            col = jnp.sum(jnp.where(r == c, css[i], 0.0), axis=1, keepdims=True)
            dcol = dcol + jnp.where(lane == head_index(*heads[i]), col, 0.0)
        dc_ref[krows, :] -= dcol
        for pp in range(pg):
            base = 3 * pp * LANES
            dproj_ref[krows, base + LANES:base + 2 * LANES] = dks[pp].astype(BF16)
            dproj_ref[krows, base + 2 * LANES:base + 3 * LANES] = dvs[pp].astype(BF16)

        @pl.when(kj == NQ - 1)
        def _():
            for pp in range(pg):
                dproj_ref[:, 3 * pp * LANES:(3 * pp + 1) * LANES] = dq_ref[pp].astype(BF16)

        if exchange is not None:
            @pl.when((grp == NPAIR // pg // 2) & (kj == 0))
            def _():
                exchange.middle(*exchange.split(ex_refs))

            @pl.when((grp == NPAIR // pg - 1) & (kj == NQ - 1))
            def _():
                exchange.rest(*exchange.split(ex_refs))

    qkv_specs = []
    for pp in range(pg):
        qkv_specs.append(pl.BlockSpec((S, LANES), lambda g, j, pp=pp: (0, 3 * (pg * g + pp))))
        qkv_specs.append(pl.BlockSpec((BQ, LANES), lambda g, j, pp=pp: (j, 3 * (pg * g + pp) + 1)))
        qkv_specs.append(pl.BlockSpec((BQ, LANES), lambda g, j, pp=pp: (j, 3 * (pg * g + pp) + 2)))
    pair = pl.BlockSpec((pg, S, 2), lambda g, j: (g, 0, 0))
    res = pl.pallas_call(
        body, name="fox_bwd", grid=(NPAIR // pg, NQ),
        in_specs=qkv_specs + [pl.BlockSpec((S, pg * LANES), lambda g, j: (0, g)), pl.BlockSpec((S, LANES), lambda g, j: (0, 0)),
                              pair, pair] + ex_in_specs,
        out_specs=[pl.BlockSpec((S, 3 * pg * LANES), lambda g, j: (0, g)), pl.BlockSpec((S, LANES), lambda g, j: (0, 0))]
        + ex_out_specs,
        out_shape=[jax.ShapeDtypeStruct((S, DPROJ_PAD), BF16), jax.ShapeDtypeStruct((S, LANES), F32)] + ex_out_shape,
        scratch_shapes=[pltpu.VMEM((2 * pg, S, LANES), BF16), pltpu.VMEM((pg, S, LANES), F32)] + ex_scratch,
        compiler_params=_cparams(),
    )(*([proj] * (3 * pg)), do, gates, lse, delta, *ex_in)
    return res[0], res[1], res[2:]


NQKV = 3 * NGH
GDN_QSCALE = GHD ** -0.5


def _shift_down(x, s):
    if s == 0:
        return x
    row = lax.broadcasted_iota(jnp.int32, x.shape, 0)
    return jnp.where(row >= s, pltpu.roll(x, s, 0), 0.0)


def _shift_up(x, s):
    if s == 0:
        return x
    n = x.shape[0]
    row = lax.broadcasted_iota(jnp.int32, x.shape, 0)
    return jnp.where(row < n - s, pltpu.roll(x, n - s, 0), 0.0)


def _conv_taps(xv):
    return [_shift_down(xv, CONV_K - 1 - j) for j in range(CONV_K)]


def _conv_pre(taps, wv):
    pre = taps[CONV_K - 1] * wv[CONV_K - 1:CONV_K, :]
    for j in range(CONV_K - 1):
        pre = pre + taps[j] * wv[j:j + 1, :]
    return pre


def _l2_factors(b):
    return b < 2 * NGH, jnp.where(b < NGH, GDN_QSCALE, 1.0)


def _gdn_pre(proj, conv_w):
    def body(x_ref, w_ref, o_ref):
        b = pl.program_id(0)
        c = _silu(_conv_pre(_conv_taps(x_ref[...]), w_ref[...]))
        normed, scale = _l2_factors(b)
        rs = lax.rsqrt(jnp.sum(c * c, axis=-1, keepdims=True) + EPS)
        o_ref[...] = c * jnp.where(normed, rs, 1.0) * scale

    return pl.pallas_call(
        body, name="gdn_pre", grid=(NQKV,),
        in_specs=[pl.BlockSpec((S, GHD), lambda b: (0, BLK_GDN + b)), pl.BlockSpec((CONV_K, GHD), lambda b: (0, b))],
        out_specs=pl.BlockSpec((S, GHD), lambda b: (0, b)),
        out_shape=jax.ShapeDtypeStruct((S, NQKV * GHD), F32), compiler_params=_cparams(),
    )(proj, conv_w)


def _gdn_pre_bwd(proj, conv_w, dqkv, dproj):
    def body(x_ref, w_ref, dy_ref, dproj_in, dx_ref, dw_ref):
        del dproj_in
        b = pl.program_id(0)
        taps = _conv_taps(x_ref[...])
        wv = w_ref[...]
        pre = _conv_pre(taps, wv)
        sig = _sigmoid(pre)
        c = pre * sig
        normed, scale = _l2_factors(b)
        g = dy_ref[...] * scale
        rs = lax.rsqrt(jnp.sum(c * c, axis=-1, keepdims=True) + EPS)
        dc_n = rs * g - c * (rs * rs * rs) * jnp.sum(g * c, axis=-1, keepdims=True)
        dc = jnp.where(normed, dc_n, g)
        dpre = dc * sig * (1.0 + pre * (1.0 - sig))
        dx = dpre * wv[CONV_K - 1:CONV_K, :]
        for j in range(CONV_K - 1):
            dx = dx + _shift_up(dpre, CONV_K - 1 - j) * wv[j:j + 1, :]
        dx_ref[...] = dx.astype(BF16)
        for j in range(CONV_K):
            dw_ref[j:j + 1, :] = jnp.sum(dpre * taps[j], axis=0, keepdims=True)

    return pl.pallas_call(
        body, name="gdn_pre_bwd", grid=(NQKV,),
        in_specs=[pl.BlockSpec((S, GHD), lambda b: (0, BLK_GDN + b)), pl.BlockSpec((CONV_K, GHD), lambda b: (0, b)),
                  pl.BlockSpec((None, S, GHD), lambda b: (b // NGH, 0, b % NGH)), pl.BlockSpec(memory_space=pl.ANY)],
        out_specs=[pl.BlockSpec((S, GHD), lambda b: (0, BLK_GDN + b)), pl.BlockSpec((CONV_K, GHD), lambda b: (0, b))],
        out_shape=[jax.ShapeDtypeStruct((S, DPROJ_PAD), BF16), jax.ShapeDtypeStruct((CONV_K, NQKV * GHD), F32)],
        input_output_aliases={3: 0}, compiler_params=_cparams(),
    )(proj, conv_w, dqkv, dproj)


CB = 16
NCB = NCH // CB


def _chunk_prep(qs, ks, vs, gcols, bcols, t_saved=None):
    n = range(len(qs))
    r = lax.broadcasted_iota(jnp.int32, (CHUNK, CHUNK), 0)
    c = lax.broadcasted_iota(jnp.int32, (CHUNK, CHUNK), 1)
    incl = c <= r
    eye = (r == c).astype(F32)
    grow = [jnp.sum(gcols[i] * eye, axis=0, keepdims=True) for i in n]
    gc_col = [jnp.sum(jnp.where(incl, grow[i], 0.0), axis=1, keepdims=True) for i in n]
    gc_row = [jnp.sum(jnp.where(r <= c, gcols[i], 0.0), axis=0, keepdims=True) for i in n]
    decay = [jnp.exp(jnp.where(incl, gc_col[i] - gc_row[i], -jnp.inf)) for i in n]
    kb = [ks[i] * bcols[i] for i in n]
    vb = [vs[i] * bcols[i] for i in n]
    kk = [_mm_nt(kb[i], ks[i]) for i in n]
    m = [jnp.where(c < r, kk[i] * decay[i], 0.0) for i in n]
    if t_saved is None:
        t_inv = [eye - m[i] for i in n]
        p = [_dot3(m[i], m[i]) for i in n]
        for step in range(5):
            t_inv = [t_inv[i] + _dot3(t_inv[i], p[i]) for i in n]
            if step < 4:
                p = [_dot3(p[i], p[i]) for i in n]
    else:
        t_inv = [_saved_inverse(m[i], t_saved[i]) for i in n]
    egc = [jnp.exp(gc_col[i]) for i in n]
    u = [_mm_nn(t_inv[i], vb[i]) for i in n]
    w = [_mm_nn(t_inv[i], kb[i] * egc[i]) for i in n]
    qk = [_mm_nt(qs[i], ks[i]) for i in n]
    gc_last = [gc_col[i][CHUNK - 1:CHUNK, :] for i in n]
    return [(u[i], w[i], qk[i] * decay[i], qs[i] * egc[i], ks[i] * jnp.exp(gc_last[i] - gc_col[i]), jnp.exp(gc_last[i]),
             t_inv[i]) for i in n]


def _prep_specs():
    rows = CB * CHUNK
    qs = pl.BlockSpec((rows, GHD), lambda i, h: (i, h))
    ks = pl.BlockSpec((rows, GHD), lambda i, h: (i, NGH + h))
    vs = pl.BlockSpec((rows, GHD), lambda i, h: (i, 2 * NGH + h))
    gs = pl.BlockSpec((rows, LANES), lambda i, h: (i, 0))
    a_s = pl.BlockSpec((None, rows, CHUNK), lambda i, h: (h, i, 0))
    gl_s = pl.BlockSpec((None, CB, 1, LANES), lambda i, h: (h, i, 0, 0))
    return qs, ks, vs, gs, a_s, gl_s


def _gdn_prep(qkv, gates, exchange=None):
    ex_in, ex_in_specs, ex_out_specs, ex_out_shape, ex_scratch = _hosted(exchange)

    def body(*refs):
        q_ref, k_ref, v_ref, g_ref = refs[:4]
        u_ref, w_ref, qd_ref, kd_ref, a_ref, gl_ref, t_ref = refs[4 + len(ex_in):11 + len(ex_in)]
        ex_refs = refs[4:4 + len(ex_in)] + refs[11 + len(ex_in):]
        h = pl.program_id(1)

        if exchange is not None:
            @pl.when((pl.program_id(0) == 0) & (h == 0))
            def _():
                exchange.start(*exchange.split(ex_refs))

        chunks = [pl.ds(cidx * CHUNK, CHUNK) for cidx in range(CB)]
        gts = [g_ref[rows, :] for rows in chunks]
        outs = _chunk_prep([q_ref[rows, :] for rows in chunks], [k_ref[rows, :] for rows in chunks],
                           [v_ref[rows, :] for rows in chunks], [_lane_col(gt, LANE_G + h) for gt in gts],
                           [_lane_col(gt, LANE_BETA + h) for gt in gts])
        for cidx, rows in enumerate(chunks):
            u, w, a, qd, kd, gl, t_inv = outs[cidx]
            u_ref[rows, :] = u
            w_ref[rows, :] = w
            qd_ref[rows, :] = qd
            kd_ref[rows, :] = kd
            a_ref[rows, :] = a
            t_ref[rows, :] = t_inv
            gl_ref[cidx] = jnp.broadcast_to(gl, (1, LANES))

        if exchange is not None:
            @pl.when((pl.program_id(0) == NCB // 2) & (h == 0))
            def _():
                exchange.middle(*exchange.split(ex_refs))

            @pl.when((pl.program_id(0) == NCB - 1) & (h == NGH - 1))
            def _():
                exchange.rest(*exchange.split(ex_refs))

    qs, ks, vs, gs, a_s, gl_s = _prep_specs()
    tok = jax.ShapeDtypeStruct((S, DGDN), F32)
    sq = jax.ShapeDtypeStruct((NGH, S, CHUNK), F32)
    res = pl.pallas_call(
        body, name="gdn_prep", grid=(NCB, NGH), in_specs=[qs, ks, vs, gs] + ex_in_specs,
        out_specs=[qs, qs, qs, qs, a_s, gl_s, a_s] + ex_out_specs,
        out_shape=[tok, tok, tok, tok, sq, jax.ShapeDtypeStruct((NGH, NCH, 1, LANES), F32), sq] + ex_out_shape,
        scratch_shapes=ex_scratch, compiler_params=_cparams(),
    )(qkv, qkv, qkv, gates, *ex_in)
    return res[:7], res[7:]


def _gdn_prep_bwd(qkv, gates, t_inv, du, dw, dqd, dkd, da, dgl, exchange=None):
    ex_in, ex_in_specs, ex_out_specs, ex_out_shape, ex_scratch = _hosted(exchange)

    def body(*refs):
        q_ref, k_ref, v_ref, g_ref, t_ref, du_ref, dw_ref, dqd_ref, dkd_ref, da_ref, dgl_ref = refs[:11]
        dqkv_ref, dg_ref = refs[11 + len(ex_in):13 + len(ex_in)]
        ex_refs = refs[11:11 + len(ex_in)] + refs[13 + len(ex_in):]
        h = pl.program_id(1)

        if exchange is not None:
            @pl.when((pl.program_id(0) == 0) & (h == 0))
            def _():
                exchange.start(*exchange.split(ex_refs))

        @pl.when(h == 0)
        def _():
            dg_ref[...] = jnp.zeros_like(dg_ref)

        lane = lax.broadcasted_iota(jnp.int32, (CHUNK, LANES), 1)
        chunks = [pl.ds(cidx * CHUNK, CHUNK) for cidx in range(CB)]
        gts = [g_ref[rows, :] for rows in chunks]
        t_saved = [t_ref[rows, :] for rows in chunks]
        _, vjp = jax.vjp(lambda *args: [o[:6] for o in _chunk_prep(*args, t_saved=t_saved)],
                         [q_ref[rows, :] for rows in chunks], [k_ref[rows, :] for rows in chunks],
                         [v_ref[rows, :] for rows in chunks], [_lane_col(gt, LANE_G + h) for gt in gts],
                         [_lane_col(gt, LANE_BETA + h) for gt in gts])
        dqs, dks, dvs, dgcs, dbcs = vjp([(du_ref[rows, :], dw_ref[rows, :], da_ref[rows, :], dqd_ref[rows, :],
                                          dkd_ref[rows, :], dgl_ref[cidx][:, 0:1]) for cidx, rows in enumerate(chunks)])
        for cidx, rows in enumerate(chunks):
            dq, dk, dv, dgc, dbc = dqs[cidx], dks[cidx], dvs[cidx], dgcs[cidx], dbcs[cidx]
            dqkv_ref[0, rows, :] = dq
            dqkv_ref[1, rows, :] = dk
            dqkv_ref[2, rows, :] = dv
            dg_ref[rows, :] += jnp.where(lane == LANE_G + h, dgc, 0.0) + jnp.where(lane == LANE_BETA + h, dbc, 0.0)

        if exchange is not None:
            @pl.when((pl.program_id(0) == NCB - 1) & (h == NGH - 1))
            def _():
                exchange.finish(*exchange.split(ex_refs))

    qs, ks, vs, gs, a_s, gl_s = _prep_specs()
    res = pl.pallas_call(
        body, name="gdn_prep_bwd", grid=(NCB, NGH), in_specs=[qs, ks, vs, gs, a_s, qs, qs, qs, qs, a_s, gl_s] + ex_in_specs,
        out_specs=[pl.BlockSpec((3, CB * CHUNK, GHD), lambda i, h: (0, i, h)), gs] + ex_out_specs,
        out_shape=[jax.ShapeDtypeStruct((3, S, DGDN), F32), jax.ShapeDtypeStruct((S, LANES), F32)] + ex_out_shape,
        scratch_shapes=ex_scratch, compiler_params=_cparams(),
    )(qkv, qkv, qkv, gates, t_inv, du, dw, dqd, dkd, da, dgl, *ex_in)
    return res[0], res[1], res[2:]


def _scan_specs(nh, parts, reverse):
    wide, rows, chunks = nh * GHD, S // parts, NCH // parts

    def part(p):
        return parts - 1 - p if reverse else p

    hs = pl.BlockSpec((rows, wide), lambda g, p: (part(p), g))
    a_s = pl.BlockSpec((nh, rows, CHUNK), lambda g, p: (g, part(p), 0))
    gl_s = pl.BlockSpec((nh, chunks, 1, LANES), lambda g, p: (g, part(p), 0, 0))
    st_s = pl.BlockSpec((nh, chunks, GHD, GHD), lambda g, p: (g, part(p), 0, 0))
    gz_s = pl.BlockSpec((rows, wide), lambda g, p: (part(p), BLK_GZ // nh + g))
    mix_s = pl.BlockSpec((rows, wide), lambda g, p: (part(p), NPAIR // nh + g))
    return hs, a_s, gl_s, st_s, gz_s, mix_s


def _head_cols(hh):
    return slice(hh * GHD, (hh + 1) * GHD)


SCAN_HEADS, SCAN_PARTS = 4, 2
SCAN_HEADS_BWD, SCAN_PARTS_BWD = 4, 4


def _gdn_scan(u, w, qd, kd, a, gl, proj, w_norm, mix):
    heads = range(SCAN_HEADS)

    def body(u_ref, w_ref, qd_ref, kd_ref, a_ref, gl_ref, z_ref, wn_ref, mix_in, mix_ref, o_ref, st_ref, carry_ref):
        del mix_in

        @pl.when(pl.program_id(1) == 0)
        def _():
            carry_ref[...] = jnp.zeros_like(carry_ref)

        def step(ci, states):
            rows = pl.ds(pl.multiple_of(ci * CHUNK, CHUNK), CHUNK)
            for hh in heads:
                st_ref[hh, ci] = states[hh]
            ws = [_dot(w_ref[rows, _head_cols(hh)], states[hh]) for hh in heads]
            qs = [_dot(qd_ref[rows, _head_cols(hh)], states[hh]) for hh in heads]
            vn = [u_ref[rows, _head_cols(hh)] - ws[hh] for hh in heads]
            av = [_dot(a_ref[hh, rows, :], vn[hh]) for hh in heads]
            kv = [_dot(kd_ref[rows, _head_cols(hh)], vn[hh], 0, 0) for hh in heads]
            for hh in heads:
                o_ref[rows, _head_cols(hh)] = qs[hh] + av[hh]
            return tuple(states[hh] * gl_ref[hh, ci] + kv[hh] for hh in heads)

        last = lax.fori_loop(0, NCH // SCAN_PARTS, step, tuple(carry_ref[hh] for hh in heads))
        for hh in heads:
            carry_ref[hh] = last[hh]
            ov = o_ref[:, _head_cols(hh)]
            mix_ref[:, _head_cols(hh)] = (ov * _rms_scale(ov) * wn_ref[...] * _silu(z_ref[:, _head_cols(hh)])).astype(BF16)

    hs, a_s, gl_s, st_s, gz_s, mix_s = _scan_specs(SCAN_HEADS, SCAN_PARTS, False)
    return pl.pallas_call(
        body, name="gdn_scan", grid=(NGH // SCAN_HEADS, SCAN_PARTS),
        in_specs=[hs, hs, hs, hs, a_s, gl_s, gz_s, pl.BlockSpec((1, GHD), lambda g, p: (0, 0)),
                  pl.BlockSpec(memory_space=pl.ANY)],
        out_specs=[mix_s, hs, st_s],
        out_shape=[jax.ShapeDtypeStruct((S, D), BF16), jax.ShapeDtypeStruct((S, DGDN), F32),
                   jax.ShapeDtypeStruct((NGH, NCH, GHD, GHD), F32)],
        input_output_aliases={8: 0}, scratch_shapes=[pltpu.VMEM((SCAN_HEADS, GHD, GHD), F32)], compiler_params=_cparams(),
    )(u, w, qd, kd, a, gl, proj, w_norm, mix)


def _gdn_scan_bwd(dmix, o, proj, w_norm, u, w, qd, kd, a, gl, states, dproj, exchange=None):
    ex_in, ex_in_specs, ex_out_specs, ex_out_shape, ex_scratch = _hosted(exchange)
    groups = NGH // SCAN_HEADS_BWD

    def body(*refs):
        dy_ref, o_ref, z_ref, wn_ref, u_ref, w_ref, qd_ref, kd_ref, a_ref, gl_ref, st_ref = refs[:11]
        dz_ref, du_ref, dw_ref, dqd_ref, dkd_ref, da_ref, dgl_ref, dwn_ref = refs[12 + len(ex_in):20 + len(ex_in)]
        do_ref, carry_ref = refs[20 + len(ex_in) + len(ex_out_shape):22 + len(ex_in) + len(ex_out_shape)]
        ex_refs = refs[12:12 + len(ex_in)] + refs[20 + len(ex_in):20 + len(ex_in) + len(ex_out_shape)] + refs[-2:]
        heads = range(SCAN_HEADS_BWD)
        chunks = NCH // SCAN_PARTS_BWD

        if exchange is not None:
            @pl.when((pl.program_id(0) == 0) & (pl.program_id(1) == 0))
            def _():
                exchange.start(*exchange.split(ex_refs))

        @pl.when((pl.program_id(0) == 0) & (pl.program_id(1) == 0))
        def _():
            dwn_ref[...] = jnp.zeros_like(dwn_ref)

        @pl.when(pl.program_id(1) == 0)
        def _():
            carry_ref[...] = jnp.zeros_like(carry_ref)

        wn = wn_ref[...]
        for hh in heads:
            c = _head_cols(hh)
            ov = o_ref[:, c]
            zv = z_ref[:, c]
            g = dy_ref[:, c]
            sig = _sigmoid(zv)
            dz_ref[:, c] = (g * (ov * _rms_scale(ov) * wn) * sig * (1.0 + zv * (1.0 - sig))).astype(BF16)
            do, dwt = _rms_bwd(ov, wn, g * zv * sig)
            do_ref[:, c] = do
            dwn_ref[...] += jnp.sum(dwt, axis=0, keepdims=True)

        def step(t, dstates):
            ci = chunks - 1 - t
            rows = pl.ds(pl.multiple_of(ci * CHUNK, CHUNK), CHUNK)
            cols = [_head_cols(hh) for hh in heads]
            state = [st_ref[hh, ci] for hh in heads]
            dov = [do_ref[rows, cols[hh]] for hh in heads]
            wv = [w_ref[rows, cols[hh]] for hh in heads]
            ws = [_dot(wv[hh], state[hh]) for hh in heads]
            adov = [_dot(a_ref[hh, rows, :], dov[hh], 0, 0) for hh in heads]
            kds = [_dot(kd_ref[rows, cols[hh]], dstates[hh]) for hh in heads]
            dqd = [_dot(dov[hh], state[hh], 1, 1) for hh in heads]
            qdo = [_dot(qd_ref[rows, cols[hh]], dov[hh], 0, 0) for hh in heads]
            vn = [u_ref[rows, cols[hh]] - ws[hh] for hh in heads]
            dvn = [adov[hh] + kds[hh] for hh in heads]
            da = [_dot(dov[hh], vn[hh], 1, 1) for hh in heads]
            dkd = [_dot(vn[hh], dstates[hh], 1, 1) for hh in heads]
            dwv = [_dot(dvn[hh], state[hh], 1, 1) for hh in heads]
            wdv = [_dot(wv[hh], dvn[hh], 0, 0) for hh in heads]
            for hh in heads:
                da_ref[hh, rows, :] = da[hh]
                dqd_ref[rows, cols[hh]] = dqd[hh]
                dkd_ref[rows, cols[hh]] = dkd[hh]
                dgl = jnp.sum(jnp.sum(dstates[hh] * state[hh], axis=1, keepdims=True), axis=0, keepdims=True)
                dgl_ref[hh, ci] = jnp.broadcast_to(dgl, (1, LANES))
                du_ref[rows, cols[hh]] = dvn[hh]
                dw_ref[rows, cols[hh]] = -dwv[hh]
            return tuple(dstates[hh] * gl_ref[hh, ci] + qdo[hh] - wdv[hh] for hh in heads)

        last = lax.fori_loop(0, chunks, step, tuple(carry_ref[hh] for hh in heads))
        for hh in heads:
            carry_ref[hh] = last[hh]

        if exchange is not None:
            @pl.when((pl.program_id(0) == groups - 1) & (pl.program_id(1) == SCAN_PARTS_BWD - 1))
            def _():
                exchange.finish(*exchange.split(ex_refs))

    hs, a_s, gl_s, st_s, gz_s, mix_s = _scan_specs(SCAN_HEADS_BWD, SCAN_PARTS_BWD, True)
    vec = pl.BlockSpec((1, GHD), lambda g, p: (0, 0))
    tok = jax.ShapeDtypeStruct((S, DGDN), F32)
    res = pl.pallas_call(
        body, name="gdn_scan_bwd", grid=(groups, SCAN_PARTS_BWD),
        in_specs=[mix_s, hs, gz_s, vec, hs, hs, hs, hs, a_s, gl_s, st_s, pl.BlockSpec(memory_space=pl.ANY)] + ex_in_specs,
        out_specs=[gz_s, hs, hs, hs, hs, a_s, gl_s, vec] + ex_out_specs,
        out_shape=[jax.ShapeDtypeStruct((S, DPROJ_PAD), BF16), tok, tok, tok, tok,
                   jax.ShapeDtypeStruct((NGH, S, CHUNK), F32), jax.ShapeDtypeStruct((NGH, NCH, 1, LANES), F32),
                   jax.ShapeDtypeStruct((1, GHD), F32)] + ex_out_shape,
        input_output_aliases={11: 0},
        scratch_shapes=[pltpu.VMEM((S // SCAN_PARTS_BWD, SCAN_HEADS_BWD * GHD), F32),
                        pltpu.VMEM((SCAN_HEADS_BWD, GHD, GHD), F32)] + ex_scratch,
        compiler_params=_cparams(),
    )(dmix, o, proj, w_norm, u, w, qd, kd, a, gl, states, dproj, *ex_in)
    return res[:8], res[8:]


def _place():
    return lax.axis_index("x"), lax.axis_index("y"), lax.axis_index("c")


def _place_scalars():
    x, y, c = _place()
    return jnp.stack([2 * x + y, c]).astype(jnp.int32)


def _other_chips(x, y):
    return [(1 - x, y), (x, 1 - y), (1 - x, 1 - y)]


HBM = pl.BlockSpec(memory_space=pltpu.HBM)
VMEM = pl.BlockSpec(memory_space=pltpu.VMEM)


def _half_rows(ref_or_rows, half):
    rows = ref_or_rows // 2
    return pl.ds(pl.multiple_of(half * rows, rows), rows)


class _Exchange:
    def __init__(self, inputs, out_shape, n_sems, start, finish=None, middle=None, rest=None):
        self.inputs, self.out_shape, self.n_sems, self.start = inputs, out_shape, n_sems, start
        if finish is None:
            def finish(*refs):
                middle(*refs)
                rest(*refs)
        self.finish = finish
        self.middle = middle if middle is not None else (lambda *refs: None)
        self.rest = rest if rest is not None else finish

    def sem_shapes(self):
        return [pltpu.SemaphoreType.DMA((self.n_sems,)), pltpu.SemaphoreType.DMA((self.n_sems,))]

    def split(self, refs):
        n_in, n_out = len(self.inputs), len(self.out_shape)
        return refs[:n_in], refs[n_in:n_in + n_out], refs[n_in + n_out], refs[n_in + n_out + 1]


def _allgather_exchange(shards, whole=()):
    n, nw = len(shards), len(whole)
    slots = 8

    def plan(src, outs, send_sems, recv_sems):
        x, y, c = _place()
        via_x, via_y, diagonal = _other_chips(x, y)
        id_x, id_y, id_diagonal = [2 * chip[0] + chip[1] for chip in (via_x, via_y, diagonal)]
        me, sibling = (x, y, c), (x, y, 1 - c)

        def rows_of(a, half, quarter):
            total = src[a].shape[0]
            if quarter is None:
                return _half_rows(total, half)
            return pl.ds(pl.multiple_of(half * (total // 2) + quarter * (total // 4), total // 4), total // 4)

        def copy(a, k, chip_index, half, quarter, to, from_src=False):
            rows = rows_of(a, half, quarter)
            dst = outs[a].at[chip_index, rows]
            return pltpu.make_async_remote_copy(
                src_ref=src[a].at[rows] if from_src else dst, dst_ref=dst, send_sem=send_sems.at[slots * a + k],
                recv_sem=recv_sems.at[slots * a + k], device_id=to, device_id_type=MESH)

        def whole_copy(b, k, chip_index, to):
            return pltpu.make_async_remote_copy(
                src_ref=src[n + b], dst_ref=outs[n + b].at[chip_index], send_sem=send_sems.at[slots * n + 3 * b + k],
                recv_sem=recv_sems.at[slots * n + 3 * b + k], device_id=to, device_id_type=MESH)

        first, stages, last = [], [], []
        for a in range(n):
            first += [copy(a, 0, 2 * x + y, c, None, (*via_x, c), True), copy(a, 1, 2 * x + y, c, None, (*via_y, c), True)]
            stages.append([
                (copy(a, 0, id_x, c, None, me),
                 [copy(a, 2, id_x, c, 0, (*via_y, c)), copy(a, 4, id_x, c, None, sibling)]),
                (copy(a, 1, id_y, c, None, me),
                 [copy(a, 3, id_y, c, 1, (*via_x, c)), copy(a, 5, id_y, c, None, sibling)]),
                (copy(a, 2, id_diagonal, c, 0, me), [copy(a, 6, id_diagonal, c, 0, sibling)]),
                (copy(a, 3, id_diagonal, c, 1, me), [copy(a, 7, id_diagonal, c, 1, sibling)]),
            ])
            last += [copy(a, 4, id_x, 1 - c, None, me), copy(a, 5, id_y, 1 - c, None, me),
                     copy(a, 6, id_diagonal, 1 - c, 0, me), copy(a, 7, id_diagonal, 1 - c, 1, me)]
        for b in range(nw):
            for k, (chip, index) in enumerate(((via_x, id_x), (via_y, id_y), (diagonal, id_diagonal))):
                first.append(whole_copy(b, k, 2 * x + y, (*chip, c)))
                last.append(whole_copy(b, k, index, me))
        return first, stages, last

    def start(*refs):
        for cp in plan(*refs)[0]:
            cp.start()

    def pass_on(stages, which):
        for stage in which:
            for per_shard in stages:
                lands, onward = per_shard[stage]
                lands.wait_recv()
                for cp in onward:
                    cp.start()

    def middle(*refs):
        pass_on(plan(*refs)[1], (0, 1))

    def rest(*refs):
        first, stages, last = plan(*refs)
        pass_on(stages, (2, 3))
        for cp in last:
            cp.wait_recv()
        for cp in first + [cp for per_shard in stages for _, onward in per_shard for cp in onward]:
            cp.wait_send()

    out_shape = [jax.ShapeDtypeStruct((NCHIP,) + s.shape, s.dtype) for s in list(shards) + list(whole)]
    return _Exchange(list(shards) + list(whole), out_shape, slots * n + 3 * nw, start, middle=middle, rest=rest)


def _with_own(gathered, own):
    x, y, _ = _place()
    return lax.dynamic_update_index_in_dim(gathered, own, 2 * x + y, axis=0)


def _simple_exchange(inputs, out_shape, copies_of):
    def start(*refs):
        for cp in copies_of(*refs):
            cp.start()

    def finish(*refs):
        for cp in copies_of(*refs):
            cp.wait()

    return _Exchange(list(inputs), out_shape, len(out_shape) * 3, start, finish)


def _pair_exchange(grads):
    def copies_of(src, outs, send_sems, recv_sems):
        x, y, c = _place()
        return [pltpu.make_async_remote_copy(
            src_ref=src[a].at[:, _half_rows(src[a].shape[1], 1 - c)], dst_ref=outs[a], send_sem=send_sems.at[a],
            recv_sem=recv_sems.at[a], device_id=(x, y, 1 - c), device_id_type=MESH) for a in range(len(src))]

    return _simple_exchange(
        grads, [jax.ShapeDtypeStruct((g.shape[0], g.shape[1] // 2, g.shape[2]), g.dtype) for g in grads], copies_of)


def _pair_sum(grads, theirs, name):
    n = len(grads)

    def body(place_ref, *refs):
        for a in range(n):
            refs[2 * n + a][...] = (refs[a][...].astype(F32) + refs[n + a][...].astype(F32)).astype(BF16)

    def specs(arrs):
        return [pl.BlockSpec((None,) + g.shape[1:], lambda j, place: (j, 0, 0)) for g in arrs]

    own_half = [pl.BlockSpec((None, g.shape[1] // 2, g.shape[2]), lambda j, place: (j, place[1], 0)) for g in grads]
    return pl.pallas_call(
        body, name=name, grid_spec=pltpu.PrefetchScalarGridSpec(
            num_scalar_prefetch=1, grid=(NCHIP,), in_specs=own_half + specs(theirs), out_specs=specs(theirs)),
        out_shape=[jax.ShapeDtypeStruct(g.shape, BF16) for g in theirs], compiler_params=_cparams(),
    )(_place_scalars(), *grads, *theirs)


def _chip_exchange(parts):
    def copies_of(src, outs, send_sems, recv_sems):
        x, y, c = _place()
        return [pltpu.make_async_remote_copy(
            src_ref=src[a].at[2 * chip[0] + chip[1]], dst_ref=outs[a].at[k], send_sem=send_sems.at[3 * a + k],
            recv_sem=recv_sems.at[3 * a + k], device_id=(*chip, c), device_id_type=MESH)
            for a in range(len(src)) for k, chip in enumerate(_other_chips(x, y))]

    return _simple_exchange(parts, [jax.ShapeDtypeStruct((NCHIP - 1,) + p.shape[1:], p.dtype) for p in parts], copies_of)


def _chip_sum(parts, received, exchange=None):
    n = len(parts)
    steps = 4
    ex_in, ex_in_specs, ex_out_specs, ex_out_shape, ex_scratch = _hosted(exchange)
    n_ex = len(ex_in)

    def body(place_ref, *refs):
        mine, theirs = refs[2 * n + n_ex:3 * n + n_ex], refs[3 * n + n_ex:4 * n + n_ex]
        ex_refs = refs[2 * n:2 * n + n_ex] + refs[4 * n + n_ex:len(refs) - n - 2]
        tiles, send_sems, recv_sems = refs[len(refs) - n - 2:len(refs) - 2], refs[-2], refs[-1]
        step = pl.program_id(0)
        if exchange is not None:
            @pl.when(step == 0)
            def _():
                exchange.start(*exchange.split(ex_refs))

        x, y, c = _place()

        def share(a, i):
            rows = tiles[a].shape[1]
            return pltpu.make_async_remote_copy(
                src_ref=tiles[a].at[i], dst_ref=theirs[a].at[pl.ds(pl.multiple_of(i * rows, rows), rows)],
                send_sem=send_sems.at[a * steps + i], recv_sem=recv_sems.at[a * steps + i], device_id=(x, y, 1 - c),
                device_id_type=MESH)

        for a in range(n):
            own, r = refs[a], refs[n + a]
            total = ((own[...].astype(F32) + r[0].astype(F32)) + r[1].astype(F32)) + r[2].astype(F32)
            mine[a][...] = total
            tiles[a][step] = total
            share(a, step).start()

        @pl.when(step == steps - 1)
        def _():
            if exchange is not None:
                exchange.finish(*exchange.split(ex_refs))
            for a in range(n):
                for i in range(steps):
                    share(a, i).wait()

    own_specs = [pl.BlockSpec((None, g.shape[1] // steps, g.shape[2]), lambda i, place: (place[0], i, 0)) for g in parts]
    received_specs = [pl.BlockSpec((g.shape[0], g.shape[1] // steps, g.shape[2]), lambda i, place: (0, i, 0))
                      for g in received]
    out_specs = [pl.BlockSpec((g.shape[1] // steps, g.shape[2]), lambda i, place: (i, 0)) for g in parts]
    halves = [jax.ShapeDtypeStruct(g.shape[1:], F32) for g in parts]
    res = pl.pallas_call(
        body, name="grads_chip_sum", grid_spec=pltpu.PrefetchScalarGridSpec(
            num_scalar_prefetch=1, grid=(steps,), in_specs=own_specs + received_specs + ex_in_specs,
            out_specs=out_specs + [HBM] * n + ex_out_specs,
            scratch_shapes=ex_scratch + [pltpu.VMEM((steps, g.shape[1] // steps, g.shape[2]), F32) for g in parts]
            + [pltpu.SemaphoreType.DMA((n * steps,))] * 2),
        out_shape=halves * 2 + ex_out_shape, compiler_params=_cparams(),
    )(_place_scalars(), *parts, *received, *ex_in)
    return res[:n], res[n:2 * n], res[2 * n:]


def _adamw_math(w, g, m, v):
    nm = ADAM_B1 * m + (1.0 - ADAM_B1) * g
    nv = ADAM_B2 * v + (1.0 - ADAM_B2) * jnp.square(g)
    m_hat = nm / (1.0 - ADAM_B1 ** ADAM_STEP)
    v_hat = nv / (1.0 - ADAM_B2 ** ADAM_STEP)
    return -ADAM_LR * (m_hat / (jnp.sqrt(v_hat) + ADAM_EPS) + ADAM_WD * w), nm, nv


def _adamw_big(ws, g_mine, g_theirs, ms, vs):
    n = len(ws)
    steps = 8
    per_half = steps // 2

    def body(place_ref, *refs):
        outs = refs[5 * n:]
        own_half = (pl.program_id(0) // per_half) == place_ref[1]
        for a in range(n):
            g = jnp.where(own_half, refs[n + a][...], refs[2 * n + a][...])
            d, nm, nv = _adamw_math(refs[a][...], g, refs[3 * n + a][...], refs[4 * n + a][...])
            outs[a][...] = g
            outs[n + a][...] = d
            outs[2 * n + a][...] = nm
            outs[3 * n + a][...] = nv

    specs = [pl.BlockSpec((w.shape[0] // steps, w.shape[1]), lambda i, place: (i, 0)) for w in ws]

    def half_specs(halves, of_this_core):
        def tile(i, place):
            first = (place[1] if of_this_core else 1 - place[1]) * per_half
            return jnp.clip(i - first, 0, per_half - 1), 0

        return [pl.BlockSpec((g.shape[0] // per_half, g.shape[1]), tile) for g in halves]

    shapes = [jax.ShapeDtypeStruct(w.shape, F32) for w in ws]
    res = pl.pallas_call(
        body, name="adamw_big", grid_spec=pltpu.PrefetchScalarGridSpec(
            num_scalar_prefetch=1, grid=(steps,),
            in_specs=specs + half_specs(g_mine, True) + half_specs(g_theirs, False) + specs * 2, out_specs=specs * 4),
        out_shape=shapes * 4, compiler_params=_cparams(),
    )(_place_scalars(), *ws, *g_mine, *g_theirs, *ms, *vs)
    return res[:n], res[n:2 * n], res[2 * n:3 * n], res[3 * n:]


def _adamw_in(w, g_mine, g_theirs, m, v):
    half = D // 2

    def body(w_ref, gm_ref, gt_ref, m_ref, v_ref, g_out, d_out, nm_out, nv_out, g_ref):
        south = lax.axis_index("c") == 0
        g_ref[0:half, :] = jnp.where(south, gm_ref[...], gt_ref[...])
        g_ref[half:D, :] = jnp.where(south, gt_ref[...], gm_ref[...])
        g = g_ref[0:CW, :]
        d, nm, nv = _adamw_math(w_ref[...], g, m_ref[...], v_ref[...])
        g_out[...] = g
        d_out[...] = d
        nm_out[...] = nm
        nv_out[...] = nv

    cols = 2 * LANES
    spec = pl.BlockSpec((CW, cols), lambda i: (0, i))
    half_spec = pl.BlockSpec((half, cols), lambda i: (0, i))
    return pl.pallas_call(
        body, name="adamw_in", grid=(D // cols,), in_specs=[spec, half_spec, half_spec, spec, spec], out_specs=[spec] * 4,
        out_shape=[jax.ShapeDtypeStruct((CW, D), F32)] * 4, scratch_shapes=[pltpu.VMEM((D, cols), F32)],
        compiler_params=_cparams(),
    )(w, g_mine, g_theirs, m, v)


NORM_NAMES = ("pre_mix_norm", "post_mix_norm", "pre_mlp_norm", "post_mlp_norm")
SMALL_NAMES = NORM_NAMES + ("gdn_conv_w", "fox_f_bias", "gdn_dt_bias", "gdn_a_log", "fox_out_norm", "gdn_out_norm")
CONV_COLS = 3 * DGDN // NCHIP


def _small_gather(d_norms, d_conv, sums, d_fox_norm, d_gdn_norm, loss_row):
    n_arrays = 6
    n_remote = n_arrays * (NDEV - 1)

    def copies_of(src, outs, send_sems, recv_sems):
        x, y, c = _place()
        me = 4 * x + 2 * y + c

        def from_me(chip_index):
            cols = pl.ds(pl.multiple_of(chip_index * CONV_COLS, LANES), CONV_COLS)
            return [src[0], src[1].at[:, cols], src[2], src[3], src[4], src[5]]

        local = [pltpu.make_async_copy(s, outs[a].at[me], send_sems.at[n_remote + a]) for a, s in enumerate(from_me(2 * x + y))]
        remote = []
        for k in range(1, NDEV):
            px, py, pc = x ^ ((k >> 2) & 1), y ^ ((k >> 1) & 1), c ^ (k & 1)
            remote += [pltpu.make_async_remote_copy(
                src_ref=s, dst_ref=outs[a].at[me], send_sem=send_sems.at[n_arrays * (k - 1) + a],
                recv_sem=recv_sems.at[n_arrays * (k - 1) + a], device_id=(px, py, pc), device_id_type=MESH)
                for a, s in enumerate(from_me(2 * px + py))]
        return local + remote

    def start(*refs):
        for cp in copies_of(*refs):
            cp.start()

    def finish(*refs):
        for cp in copies_of(*refs):
            cp.wait()

    shapes = [(4, D), (CONV_K, CONV_COLS), (8, LANES), (1, LANES), (1, LANES), (1, LANES)]
    return _Exchange([d_norms, d_conv, sums, d_fox_norm, d_gdn_norm, loss_row],
                     [jax.ShapeDtypeStruct((NDEV,) + s, F32) for s in shapes], n_remote + n_arrays, start, finish)


def _small_adamw(gathered, ws, ms, vs):
    n = len(SMALL_NAMES)
    ng = len(gathered)

    def body(*refs):
        def total(buf):
            acc = buf[0]
            for i in range(1, NDEV):
                acc = acc + buf[i]
            return acc

        t_norms, t_conv, t_sums, t_fn, t_gn, t_loss = [total(r) for r in refs[:ng]]
        w_refs, m_refs, v_refs = refs[ng:ng + n], refs[ng + n:ng + 2 * n], refs[ng + 2 * n:ng + 3 * n]
        outs = refs[ng + 3 * n:]
        outs[4 * n][...] = t_loss
        grads = [t_norms[i:i + 1, :] for i in range(4)] + [
            t_conv, t_sums[0:1, 0:NFH], t_sums[1:2, 0:NGH], t_sums[2:3, 0:NGH], t_fn[:, 0:FHD], t_gn]
        for a in range(n):
            d, nm, nv = _adamw_math(w_refs[a][...], grads[a], m_refs[a][...], v_refs[a][...])
            outs[a][...] = grads[a]
            outs[n + a][...] = d
            outs[2 * n + a][...] = nm
            outs[3 * n + a][...] = nv

    def whole(arr):
        return pl.BlockSpec(arr.shape, lambda i: (0,) * arr.ndim)

    res = pl.pallas_call(
        body, name="small_adamw", grid=(1,), in_specs=[whole(t) for t in gathered] + [whole(w) for w in ws] * 3,
        out_specs=[whole(w) for w in ws] * 4 + [pl.BlockSpec((1, LANES), lambda i: (0, 0))],
        out_shape=[jax.ShapeDtypeStruct(w.shape, F32) for w in ws] * 4 + [jax.ShapeDtypeStruct((1, LANES), F32)],
        compiler_params=_cparams(),
    )(*gathered, *ws, *ms, *vs)
    return res[:n], res[n:2 * n], res[2 * n:3 * n], res[3 * n:4 * n], res[4 * n]


CW = DPROJ // NCHIP
PROJ_RUNS = tuple((part * DFOX + hp * LANES, part * DFOX + (hp + 1) * LANES, (3 * hp + part) * LANES)
                  for hp in range(NPAIR) for part in range(3)) + (
    (1536, 1544, BLK_SMALL * LANES), (1544, 3080, BLK_GDN * LANES), (3080, 3088, BLK_SMALL * LANES + 8),
    (3088, 3600, BLK_GZ * LANES))


def _proj_pieces():
    pieces = []
    for lo, hi, at in PROJ_RUNS:
        while lo < hi:
            j = lo // CW
            end = min(hi, (j + 1) * CW)
            pieces.append((j, lo - j * CW, at, end - lo))
            at, lo = at + end - lo, end
    return pieces


RT = 256


def _to_padded_rows(gathered):
    def body(src_ref, out_ref, blocks_ref, rows_ref):
        blocks_ref[...] = src_ref[...].astype(F32)
        rows_ref[...] = jnp.zeros_like(rows_ref)
        for j, start, at, n in _proj_pieces():
            rows_ref[at:at + n, :] = blocks_ref[j, start:start + n, :]
        out_ref[...] = rows_ref[...].astype(out_ref.dtype)

    return pl.pallas_call(
        body, name="proj_rows_in", grid=(D // RT,), in_specs=[pl.BlockSpec((NCHIP, D, RT), lambda i: (0, 0, i))],
        out_specs=pl.BlockSpec((DPROJ_PAD, RT), lambda i: (0, i)), out_shape=jax.ShapeDtypeStruct((DPROJ_PAD, D), gathered.dtype),
        scratch_shapes=[pltpu.VMEM((NCHIP, D, RT), F32), pltpu.VMEM((DPROJ_PAD, RT), F32)], compiler_params=_cparams(),
    )(gathered)


def _from_padded_rows_pair_sum(w):
    steps = D // RT
    half = D // 2

    def body(src_ref, out_ref, rows_ref, blocks_ref, mine_ref, send_ref, recv_ref, send_sems, recv_sems):
        i = pl.program_id(0)
        x, y, c = _place()

        def share(t):
            return pltpu.make_async_remote_copy(
                src_ref=send_ref.at[t], dst_ref=recv_ref.at[t], send_sem=send_sems.at[t], recv_sem=recv_sems.at[t],
                device_id=(x, y, 1 - c), device_id_type=MESH)

        def rows_of(core):
            return blocks_ref[:, pl.ds(pl.multiple_of(core * half, half), half), :]

        @pl.when(i < steps)
        def _():
            rows_ref[...] = src_ref[...].astype(F32)
            blocks_ref[...] = jnp.zeros_like(blocks_ref)
            for j, start, at, n in _proj_pieces():
                blocks_ref[j, start:start + n, :] = rows_ref[at:at + n, :]
            mine_ref[i % 2] = rows_of(c)
            send_ref[i] = rows_of(1 - c).astype(BF16)
            share(i).start()

        @pl.when(i > 0)
        def _():
            share(i - 1).wait_recv()
            out_ref[...] = (mine_ref[(i - 1) % 2] + recv_ref[i - 1].astype(F32)).astype(BF16)

        @pl.when(i == steps)
        def _():
            for t in range(steps):
                share(t).wait_send()

    tile = (NCHIP, half, RT)
    return pl.pallas_call(
        body, name="proj_rows_out_pair_sum", grid=(steps + 1,),
        in_specs=[pl.BlockSpec((DPROJ_PAD, RT), lambda i: (0, jnp.minimum(i, steps - 1)))],
        out_specs=pl.BlockSpec(tile, lambda i: (0, 0, jnp.maximum(i - 1, 0))),
        out_shape=jax.ShapeDtypeStruct((NCHIP, half, D), BF16),
        scratch_shapes=[pltpu.VMEM((DPROJ_PAD, RT), F32), pltpu.VMEM((NCHIP, D, RT), F32), pltpu.VMEM((2,) + tile, F32),
                        pltpu.VMEM((steps,) + tile, BF16), pltpu.VMEM((steps,) + tile, BF16),
                        pltpu.SemaphoreType.DMA((steps,)), pltpu.SemaphoreType.DMA((steps,))],
        compiler_params=_cparams(),
    )(w)


def _local_step(x, target, first_weights, late_weights, reduce_late, reduce_in, pre_mix_norm, fox_f_bias, fox_out_norm,
                gdn_a_log, gdn_dt_bias, gdn_out_norm, post_mix_norm, pre_mlp_norm, post_mlp_norm):
    bias_vec = jnp.zeros((1, LANES), F32).at[0, 0:NFH].set(fox_f_bias).at[0, LANE_G:LANE_G + NGH].set(gdn_dt_bias)
    alog_vec = jnp.zeros((1, LANES), F32).at[0, LANE_G:LANE_G + NGH].set(gdn_a_log)
    w2 = jnp.concatenate([fox_out_norm, fox_out_norm], axis=1)

    h, first = _pre_norm(x, pre_mix_norm, exchange=first_weights[0])
    win_p, conv_w = first_weights[1](first)
    proj = _matmul(h, win_p, tb=True, tm=2048, tn=768, tk=1024, name="mm_proj", exchange=late_weights[0])
    proj, late_a = proj if late_weights[0] is not None else (proj, [])
    gates = _gates(proj, bias_vec, alog_vec)
    mix, fox_o, lse, late_b = _fox_fwd(proj, gates, w2, exchange=late_weights[1])
    qkv = _gdn_pre(proj, conv_w)
    (u, w, qd, kd, a_intra, gl, t_inv), _ = _gdn_prep(qkv, gates)
    wout, wup3 = late_weights[3](late_a, late_b)
    mix, gdn_raw, states = _gdn_scan(u, w, qd, kd, a_intra, gl, proj, gdn_out_norm, mix)
    def post_mix(acc, xv, w_post, w_pre_mlp):
        x1v = xv + acc * _rms_scale(acc) * w_post
        return acc, x1v, x1v * _rms_scale(x1v) * w_pre_mlp

    mixed, x1, h2 = _matmul(mix, wout, tm=512, tn=D, tk=1024, out_dtypes=(F32, F32, BF16), name="mm_out",
                            extra=(x, post_mix_norm, pre_mlp_norm), epilogue=post_mix)

    def relu2(acc):
        r = jnp.maximum(acc, 0.0)
        return r, r * r

    up_act = _matmul(h2, wup3, b3=True, tm=1024, tn=1024, tk=1024, out_dtypes=(BF16, BF16), epilogue=relu2,
                     name="mm_up", exchange=late_weights[2])
    (up_relu, act), late_c = up_act if late_weights[2] is not None else (up_act, [])
    wdown = late_weights[4](late_c)
    def loss_head(acc, x1v, tv, w):
        err = x1v + acc * _rms_scale(acc) * w - tv
        dx2v = err * (1.0 / D)
        dyv, dwt = _rms_bwd(acc, w, dx2v)
        part = 0.5 * jnp.sum(jnp.mean(err * err, axis=-1, keepdims=True), axis=0, keepdims=True)
        return dx2v, dyv, jnp.sum(dwt, axis=0, keepdims=True), jnp.broadcast_to(part, (1, D))

    dx2, dy, d_post_mlp, loss_wide = _matmul(
        act, wdown, tm=512, tn=D, tk=DFF, out_dtypes=(F32, BF16, F32, F32), extra=(x1, target, post_mlp_norm),
        epilogue=loss_head, n_sums=2, name="mm_down")
    loss_row = loss_wide[:, :LANES]

    dwdown = _matmul(act, dy, ta=True, tm=1024, tn=1024, tk=2048, out_dtypes=(BF16,), name="mm_dwdown")

    def relu2_bwd(acc, r):
        return (acc * 2.0 * r.astype(F32),)

    dup = _matmul(dy, wdown, tb=True, tm=1024, tn=1024, tk=1024, out_dtypes=(BF16,), extra=(up_relu,), epilogue=relu2_bwd,
                  name="mm_dact")
    dwup3 = _matmul(h2, dup, ta=True, tm=1024, tn=1024, tk=2048, out_dtypes=(BF16,), o3=True, name="mm_dwup")
    def mid_bwd(acc, x1v, dx2v, mixedv, w_pre_mlp, w_post):
        dxa, dwm = _rms_bwd(x1v, w_pre_mlp, acc)
        dx1v = dx2v + dxa
        dm, dwp = _rms_bwd(mixedv, w_post, dx1v)
        return dx1v, dm, jnp.sum(dwm, axis=0, keepdims=True), jnp.sum(dwp, axis=0, keepdims=True)

    dx1, dmixed, d_pre_mlp, d_post_mix = _matmul(
        dup, wup3, tb=True, b3=True, tm=512, tn=D, tk=DFF, out_dtypes=(F32, BF16, F32, F32),
        extra=(x1, dx2, mixed, pre_mlp_norm, post_mix_norm), epilogue=mid_bwd, n_sums=2, name="mm_dh2")
    dwout = _matmul(mix, dmixed, ta=True, tm=256, tn=1024, tk=2048, out_dtypes=(BF16,), name="mm_dwout")
    dmix = _matmul(dmixed, wout, tb=True, tm=512, tn=1024, tk=1024, name="mm_dmix")

    dfox, delta, d_fox_norm, from_sibling = _fox_norm_bwd(fox_o, dmix, w2, exchange=reduce_late[0](dwout, dwup3, dwdown))
    dproj, dcum_fox, reduced_a = _fox_bwd(proj, dfox, gates, lse, delta, exchange=reduce_late[1](from_sibling))
    (dproj, du, dw, dqd, dkd, da, dgl, d_gdn_norm), reduced_b = _gdn_scan_bwd(
        dmix, gdn_raw, proj, gdn_out_norm, u, w, qd, kd, a_intra, gl, states, dproj, exchange=reduce_late[2]())
    dqkv, dgates_gdn, reduced_c = _gdn_prep_bwd(qkv, gates, t_inv, du, dw, dqd, dkd, da, dgl, exchange=reduce_late[3]())
    reduced_late = (reduced_a, reduced_b, reduced_c)
    dproj, d_conv = _gdn_pre_bwd(proj, conv_w, dqkv, dproj)
    dproj, sums = _gates_bwd(proj, bias_vec, alog_vec, dgates_gdn, dcum_fox, dproj)

    dwin_p = _matmul(dproj, h, ta=True, tm=1280, tn=1024, tk=2048, out_dtypes=(BF16,), name="mm_dwin")
    exchange_in = reduce_in(dwin_p)
    dh = _matmul(dproj, win_p, tm=1024, tk=DPROJ_PAD, name="mm_dh", exchange=exchange_in)
    dh, reduced_in = dh if exchange_in is not None else (dh, [])
    grad_x, d_pre_mix = _pre_norm_bwd(dh, x, pre_mix_norm, dx1)

    d_norms = jnp.concatenate([d_pre_mix, d_post_mix, d_pre_mlp, d_post_mlp], axis=0)
    return grad_x, (d_norms, d_conv, sums, d_fox_norm, d_gdn_norm, loss_row), reduced_late, reduced_in


def kernel(x, pre_mix_norm, w_in, fox_f_bias, fox_out_norm, gdn_conv_w, gdn_a_log, gdn_dt_bias, gdn_out_norm, w_out, post_mix_norm, pre_mlp_norm, w_up, w_down, post_mlp_norm, loss_target, m_pre_mix_norm, m_w_in, m_fox_f_bias, m_fox_out_norm, m_gdn_conv_w, m_gdn_a_log, m_gdn_dt_bias, m_gdn_out_norm, m_w_out, m_post_mix_norm, m_pre_mlp_norm, m_w_up, m_w_down, m_post_mlp_norm, v_pre_mix_norm, v_w_in, v_fox_f_bias, v_fox_out_norm, v_gdn_conv_w, v_gdn_a_log, v_gdn_dt_bias, v_gdn_out_norm, v_w_out, v_post_mix_norm, v_pre_mlp_norm, v_w_up, v_w_down, v_post_mlp_norm):
    weights = dict(pre_mix_norm=pre_mix_norm, w_in=w_in, fox_f_bias=fox_f_bias, fox_out_norm=fox_out_norm, gdn_conv_w=gdn_conv_w,
                   gdn_a_log=gdn_a_log, gdn_dt_bias=gdn_dt_bias, gdn_out_norm=gdn_out_norm, w_out=w_out, post_mix_norm=post_mix_norm,
                   pre_mlp_norm=pre_mlp_norm, w_up=w_up, w_down=w_down, post_mlp_norm=post_mlp_norm)
    m_in = dict(pre_mix_norm=m_pre_mix_norm, w_in=m_w_in, fox_f_bias=m_fox_f_bias, fox_out_norm=m_fox_out_norm, gdn_conv_w=m_gdn_conv_w,
                gdn_a_log=m_gdn_a_log, gdn_dt_bias=m_gdn_dt_bias, gdn_out_norm=m_gdn_out_norm, w_out=m_w_out, post_mix_norm=m_post_mix_norm,
                pre_mlp_norm=m_pre_mlp_norm, w_up=m_w_up, w_down=m_w_down, post_mlp_norm=m_post_mlp_norm)
    v_in = dict(pre_mix_norm=v_pre_mix_norm, w_in=v_w_in, fox_f_bias=v_fox_f_bias, fox_out_norm=v_fox_out_norm, gdn_conv_w=v_gdn_conv_w,
                gdn_a_log=v_gdn_a_log, gdn_dt_bias=v_gdn_dt_bias, gdn_out_norm=v_gdn_out_norm, w_out=v_w_out, post_mix_norm=v_post_mix_norm,
                pre_mlp_norm=v_pre_mlp_norm, w_up=v_w_up, w_down=v_w_down, post_mlp_norm=v_post_mlp_norm)
    order_w = ("pre_mix_norm", "w_in", "fox_f_bias", "fox_out_norm", "gdn_conv_w", "gdn_a_log", "gdn_dt_bias", "gdn_out_norm", "w_out",
               "post_mix_norm", "pre_mlp_norm", "w_up", "w_down", "post_mlp_norm")
    big = ("w_in", "w_out", "w_up", "w_down")

    def row(v):
        return v if v.ndim == 2 else v.reshape(1, -1)

    win_shard = jnp.pad(w_in.T.astype(BF16), ((0, D - CW), (0, 0)))

    def resolve_first(gathered):
        win_g, conv_g = gathered
        return (_to_padded_rows(_with_own(win_g, win_shard)),
                _with_own(conv_g, gdn_conv_w).transpose(1, 0, 2).reshape(CONV_K, 3 * DGDN))

    late_shards = [weights[n].astype(BF16) for n in big[1:]]

    gathered_down = []

    def resolve_out_up(gathered_out, gathered_mlp):
        gathered_down.append(gathered_mlp[1])
        return _with_own(gathered_out[0], late_shards[0]).reshape(D, D), _with_own(gathered_mlp[0], late_shards[1])

    def resolve_down(_):
        return _with_own(gathered_down[0], late_shards[2]).reshape(DFF, D)

    pair_sums, late_blocks = {}, []

    def pair_summed(names, blocks, theirs):
        for n, s in zip(names, _pair_sum(blocks, theirs, "grads_pair_sum_" + names[0])):
            pair_sums[n] = s

    def late_pair_exchange(dwout, dwup3, dwdown):
        late_blocks.extend([dwout.reshape(NCHIP, D // NCHIP, D), dwup3, dwdown.reshape(NCHIP, DFF // NCHIP, D)])
        return _pair_exchange(late_blocks)

    def late_chip_exchange(theirs):
        pair_summed(big[1:], late_blocks, theirs)
        return _chip_exchange([pair_sums["w_up"], pair_sums["w_down"]])

    def reduce_in(dwin_p):
        pair_sums["w_in"] = _from_padded_rows_pair_sum(dwin_p)
        return _chip_exchange([pair_sums["w_in"]])

    grad_x, small, received_late, received_in = _local_step(
        x[0], loss_target[0], (_allgather_exchange([win_shard], whole=[gdn_conv_w]), resolve_first),
        (_allgather_exchange(late_shards[:1]), _allgather_exchange(late_shards[1:]), None, resolve_out_up, resolve_down),
        (late_pair_exchange, late_chip_exchange, lambda: None, lambda: _chip_exchange([pair_sums["w_out"]])),
        reduce_in, row(pre_mix_norm), fox_f_bias, row(fox_out_norm), gdn_a_log, gdn_dt_bias,
        row(gdn_out_norm), row(post_mix_norm), row(pre_mlp_norm), row(post_mlp_norm))
    received_mlp, _, received_out = received_late

    g_mine, g_theirs, small_gathered = _chip_sum(
        [pair_sums[n] for n in big], list(received_in[:1]) + list(received_out[:1]) + list(received_mlp[:2]),
        exchange=_small_gather(*small))

    g_big, d_big, nm_big, nv_big = _adamw_big(
        [weights[n] for n in big[1:]], g_mine[1:], g_theirs[1:], [m_in[n] for n in big[1:]], [v_in[n] for n in big[1:]])
    in_t = _adamw_in(w_in.T, g_mine[0], g_theirs[0], m_w_in.T, v_w_in.T)
    g_small, d_small, nm_small, nv_small, loss_total = _small_adamw(
        small_gathered, [row(weights[n]) for n in SMALL_NAMES], [row(m_in[n]) for n in SMALL_NAMES],
        [row(v_in[n]) for n in SMALL_NAMES])

    grads, delta, new_m, new_v = {}, {}, {}, {}
    grads["w_in"], delta["w_in"], new_m["w_in"], new_v["w_in"] = [t.T for t in in_t]
    for i, n in enumerate(big[1:]):
        grads[n], delta[n], new_m[n], new_v[n] = g_big[i], d_big[i], nm_big[i], nv_big[i]
    for i, n in enumerate(SMALL_NAMES):
        shape = weights[n].shape
        grads[n], delta[n], new_m[n], new_v[n] = (g_small[i].reshape(shape), d_small[i].reshape(shape),
                                                  nm_small[i].reshape(shape), nv_small[i].reshape(shape))
    return (loss_total[0, 0], grad_x[None], *[grads[n] for n in order_w], *[delta[n] for n in order_w], *[new_m[n] for n in order_w],
            *[new_v[n] for n in order_w])
```

```python
import jax
import jax.numpy as jnp
from jax import lax
from jax.experimental import pallas as pl
from jax.experimental.pallas import tpu as pltpu

F32 = jnp.float32
BF16 = jnp.bfloat16
MESH = pl.DeviceIdType.MESH

S = 2048
D = 1024
NFH, FHD = 8, 64
NPAIR = NFH // 2
NGH, GHD = 4, 128
DFOX = NFH * FHD
DGDN = NGH * GHD
CHUNK = 64
NCH = S // CHUNK
CONV_K = 4
DFF = 4 * D
EPS = 1e-6
DPROJ = 3600
LANES = 128
DPROJ_PAD = 3840
BLK_GDN = 12
BLK_GZ = 24
BLK_SMALL = 28
NCHIP = 4
NDEV = 8
VMEM_LIMIT = 56 * 1024 * 1024

ADAM_LR = 0.001
ADAM_B1 = 0.9
ADAM_B2 = 0.999
ADAM_EPS = 1e-08
ADAM_WD = 0.01
ADAM_STEP = 10


def _cparams(**kw):
    return pltpu.CompilerParams(vmem_limit_bytes=VMEM_LIMIT, **kw)


def _dn(ca, cb):
    return (((ca,), (cb,)), ((), ()))


def _dot(a, b, ca=1, cb=0):
    return lax.dot_general(a.astype(BF16), b.astype(BF16), _dn(ca, cb), preferred_element_type=F32)


def _hdot(a, b, ca=1, cb=0):
    return lax.dot_general(a.astype(F32), b.astype(F32), _dn(ca, cb), precision=lax.Precision.HIGHEST,
                           preferred_element_type=F32)


def _dot3(a, b, ca=1, cb=0):
    a_hi, b_hi = a.astype(BF16), b.astype(BF16)
    a_lo, b_lo = (a - a_hi.astype(F32)).astype(BF16), (b - b_hi.astype(F32)).astype(BF16)
    dn = _dn(ca, cb)
    return (lax.dot_general(a_hi, b_hi, dn, preferred_element_type=F32)
            + (lax.dot_general(a_hi, b_lo, dn, preferred_element_type=F32)
               + lax.dot_general(a_lo, b_hi, dn, preferred_element_type=F32)))


@jax.custom_vjp
def _mm_nn(a, b):
    return _dot(a, b, 1, 0)


def _mm_nn_fwd(a, b):
    return _dot(a, b, 1, 0), (a, b)


def _mm_nn_bwd(res, g):
    a, b = res
    return _dot(g, b, 1, 1), _dot(a, g, 0, 0)


_mm_nn.defvjp(_mm_nn_fwd, _mm_nn_bwd)


@jax.custom_vjp
def _mm_nt(a, b):
    return _dot(a, b, 1, 1)


def _mm_nt_fwd(a, b):
    return _dot(a, b, 1, 1), (a, b)


def _mm_nt_bwd(res, g):
    a, b = res
    return _dot(g, b, 1, 0), _dot(g, a, 0, 0)


_mm_nt.defvjp(_mm_nt_fwd, _mm_nt_bwd)


@jax.custom_vjp
def _saved_inverse(m, t_inv):
    del m
    return t_inv


def _saved_inverse_fwd(m, t_inv):
    del m
    return t_inv, t_inv


def _saved_inverse_bwd(t_inv, g):
    return -_dot3(_dot3(t_inv, g, 0, 0), t_inv, 1, 1), jnp.zeros_like(t_inv)


_saved_inverse.defvjp(_saved_inverse_fwd, _saved_inverse_bwd)


def _sigmoid(z):
    return 1.0 / (1.0 + jnp.exp(-z))


def _softplus(z):
    return jnp.maximum(z, 0.0) + jnp.log(1.0 + jnp.exp(-jnp.abs(z)))


def _silu(z):
    return z * _sigmoid(z)


def _rms_scale(x):
    return lax.rsqrt(jnp.mean(x * x, axis=-1, keepdims=True) + EPS)


def _rms_bwd(x, w, g):
    r = _rms_scale(x)
    gw = g * w
    dx = r * gw - x * (r * r * r) * jnp.mean(gw * x, axis=-1, keepdims=True)
    return dx, g * x * r


def _matmul(a, b, *, name, ta=False, tb=False, tm=512, tn=512, tk=512, out_dtypes=(F32,), b3=False, o3=False,
            extra=(), epilogue=None, exchange=None, n_sums=0):
    m, k = (a.shape[1], a.shape[0]) if ta else a.shape
    if b3:
        n = b.shape[1] if tb else b.shape[0] * b.shape[2]
        kb = b.shape[0] * b.shape[2] if tb else b.shape[1]
    else:
        n, kb = (b.shape[0], b.shape[1]) if tb else (b.shape[1], b.shape[0])
    assert kb == k, (name, kb, k)
    tm, tn, tk = min(tm, m), min(tn, n), min(tk, k)
    assert m % tm == 0 and n % tn == 0 and k % tk == 0, (name, m, n, k, tm, tn, tk)
    nk = k // tk
    whole_k_blocks = b3 and tb and not ta and nk == 1 and b.shape[0] > 1
    n_extra = len(extra)
    n_out = len(out_dtypes)
    grid = (m // tm, n // tn, nk)
    ex_in, ex_in_specs, ex_out_specs, ex_out_shape, ex_scratch = _hosted(exchange)

    def body(*refs):
        a_ref, b_ref = refs[0], refs[1]
        extra_refs = refs[2:2 + n_extra]
        first_out = 2 + n_extra + len(ex_in)
        out_refs = refs[first_out:first_out + n_out]
        ex_refs = refs[2 + n_extra:first_out] + refs[first_out + n_out:first_out + n_out + len(ex_out_shape)] + refs[-2:]
        step = [pl.program_id(d) for d in range(3)]

        if exchange is not None:
            @pl.when((step[0] == 0) & (step[1] == 0) & (step[2] == 0))
            def _():
                exchange.start(*exchange.split(ex_refs))

        def finish(acc):
            outs = (acc,) if epilogue is None else epilogue(acc, *[r[...] for r in extra_refs])
            for o_ref, val in zip(out_refs[:n_out - n_sums], outs):
                o_ref[...] = val.astype(o_ref.dtype)
            for o_ref, val in zip(out_refs[n_out - n_sums:], outs[n_out - n_sums:]):
                @pl.when(step[0] == 0)
                def _(o_ref=o_ref, val=val):
                    o_ref[...] = val

                @pl.when(step[0] > 0)
                def _(o_ref=o_ref, val=val):
                    o_ref[...] += val

        if whole_k_blocks:
            width = b.shape[2]
            part = _dot(a_ref[:, 0:width], b_ref[0], 1, 1)
            for blk in range(1, b.shape[0]):
                part = part + _dot(a_ref[:, blk * width:(blk + 1) * width], b_ref[blk], 1, 1)
        else:
            part = _dot(a_ref[...], b_ref[...], 0 if ta else 1, 1 if tb else 0)
        if nk == 1:
            finish(part)
        else:
            acc_ref = refs[first_out + n_out + len(ex_out_shape)]

            @pl.when(step[2] == 0)
            def _():
                acc_ref[...] = part

            @pl.when(step[2] > 0)
            def _():
                acc_ref[...] += part

            @pl.when(step[2] == nk - 1)
            def _():
                finish(acc_ref[...])

        if exchange is not None:
            flat = (step[0] * grid[1] + step[1]) * nk + step[2]
            total = grid[0] * grid[1] * nk

            @pl.when(flat == total // 2)
            def _():
                exchange.middle(*exchange.split(ex_refs))

            @pl.when(flat == total - 1)
            def _():
                exchange.rest(*exchange.split(ex_refs))

    a_spec = pl.BlockSpec((tk, tm), lambda i, j, kk: (kk, i)) if ta else pl.BlockSpec((tm, tk), lambda i, j, kk: (i, kk))
    if whole_k_blocks:
        b_spec = pl.BlockSpec((b.shape[0], tn, b.shape[2]), lambda i, j, kk: (0, j, 0))
    elif b3 and tb:
        assert b.shape[2] == tk
        b_spec = pl.BlockSpec((None, tn, tk), lambda i, j, kk: (kk, j, 0))
    elif b3:
        assert b.shape[2] == tn
        b_spec = pl.BlockSpec((None, tk, tn), lambda i, j, kk: (j, kk, 0))
    elif tb:
        b_spec = pl.BlockSpec((tn, tk), lambda i, j, kk: (j, kk))
    else:
        b_spec = pl.BlockSpec((tk, tn), lambda i, j, kk: (kk, j))
    tile = pl.BlockSpec((tm, tn), lambda i, j, kk: (i, j))
    out_specs = [tile] * n_out
    out_shape = [jax.ShapeDtypeStruct((m, n), dt) for dt in out_dtypes]
    if o3:
        out_specs[0] = pl.BlockSpec((None, tm, tn), lambda i, j, kk: (j, i, 0))
        out_shape[0] = jax.ShapeDtypeStruct((n // tn, m, tn), out_dtypes[0])
    assert n_sums == 0 or tn == n
    for r in range(n_out - n_sums, n_out):
        out_specs[r] = pl.BlockSpec((1, tn), lambda i, j, kk: (0, 0))
        out_shape[r] = jax.ShapeDtypeStruct((1, n), out_dtypes[r])
    res = pl.pallas_call(
        body, name=name, grid=grid,
        in_specs=[a_spec, b_spec] + [tile if e.shape[0] == m else pl.BlockSpec((1, tn), lambda i, j, kk: (0, j)) for e in extra]
        + ex_in_specs, out_specs=out_specs + ex_out_specs,
        out_shape=out_shape + ex_out_shape,
        scratch_shapes=([pltpu.VMEM((tm, tn), F32)] if nk > 1 else []) + ex_scratch,
        compiler_params=_cparams(),
    )(a, b, *extra, *ex_in)
    if exchange is not None:
        return (res[0] if n_out == 1 else res[:n_out]), res[n_out:]
    return res[0] if n_out == 1 else res


TR = 512


def _row_spec(cols):
    return pl.BlockSpec((TR, cols), lambda i: (i, 0))


def _vec_spec(cols):
    return pl.BlockSpec((1, cols), lambda i: (0, 0))


def _pre_norm(x, w, exchange=None):
    ex_in, ex_in_specs, ex_out_specs, ex_out_shape, ex_scratch = _hosted(exchange)

    def body(*refs):
        x_ref, w_ref, h_ref = refs[0], refs[1], refs[2 + len(ex_in)]
        ex_refs = refs[2:2 + len(ex_in)] + refs[3 + len(ex_in):]
        if exchange is not None:
            @pl.when(pl.program_id(0) == 0)
            def _():
                exchange.start(*exchange.split(ex_refs))

        xv = x_ref[...]
        h_ref[...] = (xv * _rms_scale(xv) * w_ref[...]).astype(BF16)

        if exchange is not None:
            @pl.when(pl.program_id(0) == S // TR - 1)
            def _():
                exchange.finish(*exchange.split(ex_refs))

    res = pl.pallas_call(
        body, name="pre_norm", grid=(S // TR,), in_specs=[_row_spec(D), _vec_spec(D)] + ex_in_specs,
        out_specs=[_row_spec(D)] + ex_out_specs, out_shape=[jax.ShapeDtypeStruct((S, D), BF16)] + ex_out_shape,
        scratch_shapes=ex_scratch, compiler_params=_cparams(),
    )(x, w, *ex_in)
    return res[0], res[1:]


def _pre_norm_bwd(dh, x, w, dx1):
    def body(dh_ref, x_ref, w_ref, dx1_ref, dx_ref, dw_ref):
        i = pl.program_id(0)
        dxa, dwt = _rms_bwd(x_ref[...], w_ref[...], dh_ref[...])
        dx_ref[...] = dx1_ref[...] + dxa

        @pl.when(i == 0)
        def _():
            dw_ref[...] = jnp.zeros_like(dw_ref)

        dw_ref[...] += jnp.sum(dwt, axis=0, keepdims=True)

    return pl.pallas_call(
        body, name="pre_norm_bwd", grid=(S // TR,),
        in_specs=[_row_spec(D), _row_spec(D), _vec_spec(D), _row_spec(D)], out_specs=[_row_spec(D), _vec_spec(D)],
        out_shape=[jax.ShapeDtypeStruct((S, D), F32), jax.ShapeDtypeStruct((1, D), F32)], compiler_params=_cparams(),
    )(dh, x, w, dx1)


BQ = 512
NQ = S // BQ
LANE_BETA, LANE_G = 8, 12


def _gate_lanes(shape):
    lane = lax.broadcasted_iota(jnp.int32, shape, 1)
    return lane < LANE_BETA, (lane >= LANE_BETA) & (lane < LANE_G), (lane >= LANE_G) & (lane < LANE_G + NGH)


def _gates(proj, bias_vec, alog_vec):
    def body(s_ref, b_ref, a_ref, o_ref, carry_ref):
        i = pl.program_id(0)

        @pl.when(i == 0)
        def _():
            carry_ref[...] = jnp.zeros_like(carry_ref)

        z = s_ref[...] + b_ref[...]
        tail = jnp.log(1.0 + jnp.exp(-jnp.abs(z)))
        sp = jnp.maximum(z, 0.0) + tail
        lf = jnp.minimum(z, 0.0) - tail
        r = lax.broadcasted_iota(jnp.int32, (BQ, BQ), 0)
        c = lax.broadcasted_iota(jnp.int32, (BQ, BQ), 1)
        tri = (c <= r).astype(F32)
        cum = _hdot(tri, lf) + carry_ref[...]
        carry_ref[...] = cum[BQ - 1:BQ, :]
        is_fox, is_beta, is_g = _gate_lanes(z.shape)
        o_ref[...] = jnp.where(is_fox, cum, jnp.where(is_beta, _sigmoid(z), jnp.where(is_g, -jnp.exp(a_ref[...]) * sp, 0.0)))

    return pl.pallas_call(
        body, name="gates", grid=(NQ,),
        in_specs=[pl.BlockSpec((BQ, LANES), lambda i: (i, BLK_SMALL)), _vec_spec(LANES), _vec_spec(LANES)],
        out_specs=pl.BlockSpec((BQ, LANES), lambda i: (i, 0)), out_shape=jax.ShapeDtypeStruct((S, LANES), F32),
        scratch_shapes=[pltpu.VMEM((1, LANES), F32)], compiler_params=_cparams(),
    )(proj, bias_vec, alog_vec)


def _gates_bwd(proj, bias_vec, alog_vec, dgates_gdn, dcum_fox, dproj):
    def body(s_ref, b_ref, a_ref, dg_ref, dc_ref, dproj_in, dproj_ref, red_ref, carry_ref):
        del dproj_in
        i = pl.program_id(0)

        @pl.when(i == 0)
        def _():
            carry_ref[...] = jnp.zeros_like(carry_ref)
            red_ref[...] = jnp.zeros_like(red_ref)

        z = s_ref[...] + b_ref[...]
        dg = dg_ref[...] + dc_ref[...]
        r = lax.broadcasted_iota(jnp.int32, (BQ, BQ), 0)
        c = lax.broadcasted_iota(jnp.int32, (BQ, BQ), 1)
        upper = (c >= r).astype(F32)
        dlf = _hdot(upper, dg) + carry_ref[...]
        carry_ref[...] = dlf[0:1, :]
        sig = _sigmoid(z)
        g_scale = -jnp.exp(a_ref[...])
        is_fox, is_beta, is_g = _gate_lanes(z.shape)
        ds = jnp.where(is_fox, dlf * (1.0 - sig), jnp.where(is_beta, dg * sig * (1.0 - sig), jnp.where(is_g, dg * g_scale * sig, 0.0)))
        dproj_ref[:, 0:LANES] = ds.astype(BF16)
        dproj_ref[:, LANES:2 * LANES] = jnp.zeros((BQ, LANES), BF16)
        dalog = jnp.where(is_g, dg * g_scale * _softplus(z), 0.0)
        sums = jnp.sum(ds, axis=0, keepdims=True)
        red_ref[0:1, :] += jnp.where(is_fox[0:1], sums, 0.0)
        red_ref[1:2, :] += pltpu.roll(jnp.where(is_g[0:1], sums, 0.0), LANES - LANE_G, 1)
        red_ref[2:3, :] += pltpu.roll(jnp.sum(dalog, axis=0, keepdims=True), LANES - LANE_G, 1)

    blk = pl.BlockSpec((BQ, LANES), lambda i: (NQ - 1 - i, 0))
    return pl.pallas_call(
        body, name="gates_bwd", grid=(NQ,),
        in_specs=[pl.BlockSpec((BQ, LANES), lambda i: (NQ - 1 - i, BLK_SMALL)), _vec_spec(LANES), _vec_spec(LANES), blk, blk,
                  pl.BlockSpec(memory_space=pl.ANY)],
        out_specs=[pl.BlockSpec((BQ, 2 * LANES), lambda i: (NQ - 1 - i, BLK_SMALL // 2)), pl.BlockSpec((8, LANES), lambda i: (0, 0))],
        out_shape=[jax.ShapeDtypeStruct((S, DPROJ_PAD), BF16), jax.ShapeDtypeStruct((8, LANES), F32)],
        input_output_aliases={5: 0},
        scratch_shapes=[pltpu.VMEM((1, LANES), F32)], compiler_params=_cparams(),
    )(proj, bias_vec, alog_vec, dgates_gdn, dcum_fox, dproj)


FOX_SCALE = FHD ** -0.5
FOX_PAIRS = 2
FOX_PAIRS_BWD = 2


def _head_mask(e):
    lane = lax.broadcasted_iota(jnp.int32, (1, LANES), 1)
    return (lane >= e * FHD) & (lane < (e + 1) * FHD)


def _lane_col(vals, index):
    lane = lax.broadcasted_iota(jnp.int32, vals.shape, 1)
    return jnp.sum(jnp.where(lane == index, vals, 0.0), axis=1, keepdims=True)


def _sublane_row(vals, index):
    row = lax.broadcasted_iota(jnp.int32, vals.shape, 0)
    return jnp.sum(jnp.where(row == index, vals, 0.0), axis=0, keepdims=True)


def _pair_cols(c0, c1):
    lane = lax.broadcasted_iota(jnp.int32, (c0.shape[0], 2), 1)
    return jnp.where(lane == 0, c0, c1)


def _split3(x):
    hi = x.astype(BF16).astype(F32)
    rest = x - hi
    mid = rest.astype(BF16).astype(F32)
    return hi, mid, (rest - mid).astype(BF16).astype(F32)


def _fox_operand(vals, e, cum, is_query):
    lane = lax.broadcasted_iota(jnp.int32, (1, LANES), 1)
    base = (1 - e) * FHD
    parts = _split3(cum)
    own = jnp.where(_head_mask(e), vals * FOX_SCALE if is_query else vals, 0.0)
    cum_at, ones_at = (base, base + 3) if is_query else (base + 3, base)
    sign = 1.0 if is_query else -1.0
    out = own + jnp.where((lane >= ones_at) & (lane < ones_at + 3), 1.0, 0.0)
    for i, part in enumerate(parts):
        out = out + jnp.where(lane == cum_at + i, sign * part, 0.0)
    return out.astype(BF16)


def _causal_block():
    return lax.broadcasted_iota(jnp.int32, (BQ, BQ), 1) <= lax.broadcasted_iota(jnp.int32, (BQ, BQ), 0)


def _head_rms(o, masks):
    o2 = o * o
    r = [lax.rsqrt(jnp.sum(jnp.where(mk, o2, 0.0), axis=1, keepdims=True) * (1.0 / FHD) + EPS) for mk in masks]
    return jnp.where(masks[0], r[0], r[1])


def _hosted(exchange):
    if exchange is None:
        return [], [], [], [], []
    return (exchange.inputs, [HBM] * len(exchange.inputs), [HBM] * len(exchange.out_shape), exchange.out_shape,
            exchange.sem_shapes())


def _fox_fwd(proj, gates, w2, exchange=None):
    ex_in, ex_in_specs, ex_out_specs, ex_out_shape, ex_scratch = _hosted(exchange)

    n_in = 3 * FOX_PAIRS + 2
    heads = [(pp, e) for pp in range(FOX_PAIRS) for e in range(2)]

    def body(*refs):
        qkv_refs, g_ref, w_ref = refs[:3 * FOX_PAIRS], refs[3 * FOX_PAIRS], refs[3 * FOX_PAIRS + 1]
        mix_ref, o_ref, lse_ref = refs[n_in + len(ex_in):n_in + 3 + len(ex_in)]
        ka_ref, vb_ref = refs[n_in + 3 + len(ex_in) + len(ex_out_shape):n_in + 5 + len(ex_in) + len(ex_out_shape)]
        ex_refs = refs[n_in:n_in + len(ex_in)] + refs[n_in + 3 + len(ex_in):n_in + 3 + len(ex_in) + len(ex_out_shape)] + refs[-2:]
        grp, qi = pl.program_id(0), pl.program_id(1)

        def head_index(pp, e):
            return 2 * (FOX_PAIRS * grp + pp) + e

        if exchange is not None:
            @pl.when((grp == 0) & (qi == 0))
            def _():
                exchange.start(*exchange.split(ex_refs))

        @pl.when(qi == 0)
        def _():
            gt = g_ref[...]
            for pp in range(FOX_PAIRS):
                kv = qkv_refs[3 * pp + 1][...]
                for e in range(2):
                    ka_ref[2 * pp + e] = _fox_operand(kv, e, _lane_col(gt, head_index(pp, e)), False)
                vb_ref[pp] = qkv_refs[3 * pp + 2][...].astype(BF16)

        masks = [_head_mask(0), _head_mask(1)]
        gt = g_ref[pl.ds(pl.multiple_of(qi * BQ, BQ), BQ), :]
        qs = [_fox_operand(qkv_refs[3 * pp][...], e, _lane_col(gt, head_index(pp, e)), True) for pp, e in heads]
        n = range(len(heads))

        def block(kj, carry, diagonal):
            rows = pl.ds(pl.multiple_of(kj * BQ, BQ), BQ)
            s = [_dot(qs[i], ka_ref[i, rows, :], 1, 1) for i in n]
            if diagonal:
                s = [jnp.where(_causal_block(), s[i], -jnp.inf) for i in n]
            m_new = [jnp.maximum(carry[i][0], jnp.max(s[i], axis=-1, keepdims=True)) for i in n]
            p = [jnp.exp(s[i] - m_new[i]) for i in n]
            alpha = [jnp.exp(carry[i][0] - m_new[i]) for i in n]
            l_new = [alpha[i] * carry[i][1] + jnp.sum(p[i], axis=-1, keepdims=True) for i in n]
            pv = [_dot(p[i], vb_ref[heads[i][0], rows, :]) for i in n]
            return tuple((m_new[i], l_new[i], alpha[i] * carry[i][2] + pv[i]) for i in n)

        one = (jnp.full((BQ, 1), -jnp.inf, F32), jnp.zeros((BQ, 1), F32), jnp.zeros((BQ, LANES), F32))
        below = lax.fori_loop(0, qi, lambda kj, carry: block(kj, carry, False), (one,) * len(heads))
        done = block(qi, below, True)
        for pp in range(FOX_PAIRS):
            (m0, l0, a0), (m1, l1, a1) = done[2 * pp], done[2 * pp + 1]
            o = jnp.where(masks[0], a0 / l0, a1 / l1)
            cols = slice(pp * LANES, (pp + 1) * LANES)
            o_ref[:, cols] = o
            mix_ref[:, cols] = (o * _head_rms(o, masks) * w_ref[...]).astype(BF16)
            lse_ref[pp] = _pair_cols(m0 + jnp.log(l0), m1 + jnp.log(l1))

        if exchange is not None:
            @pl.when((grp == NPAIR // FOX_PAIRS // 2) & (qi == 0))
            def _():
                exchange.middle(*exchange.split(ex_refs))

            @pl.when((grp == NPAIR // FOX_PAIRS - 1) & (qi == NQ - 1))
            def _():
                exchange.rest(*exchange.split(ex_refs))

    qkv_specs = []
    for pp in range(FOX_PAIRS):
        qkv_specs.append(pl.BlockSpec((BQ, LANES), lambda g, i, pp=pp: (i, 3 * (FOX_PAIRS * g + pp))))
        qkv_specs.append(pl.BlockSpec((S, LANES), lambda g, i, pp=pp: (0, 3 * (FOX_PAIRS * g + pp) + 1)))
        qkv_specs.append(pl.BlockSpec((S, LANES), lambda g, i, pp=pp: (0, 3 * (FOX_PAIRS * g + pp) + 2)))
    blk = pl.BlockSpec((BQ, FOX_PAIRS * LANES), lambda g, i: (i, g))
    res = pl.pallas_call(
        body, name="fox_fwd", grid=(NPAIR // FOX_PAIRS, NQ),
        in_specs=qkv_specs + [pl.BlockSpec((S, LANES), lambda g, i: (0, 0)), pl.BlockSpec((1, LANES), lambda g, i: (0, 0))]
        + ex_in_specs,
        out_specs=[blk, blk, pl.BlockSpec((FOX_PAIRS, BQ, 2), lambda g, i: (g, i, 0))] + ex_out_specs,
        out_shape=[jax.ShapeDtypeStruct((S, D), BF16), jax.ShapeDtypeStruct((S, DFOX), F32),
                   jax.ShapeDtypeStruct((NPAIR, S, 2), F32)] + ex_out_shape,
        scratch_shapes=[pltpu.VMEM((2 * FOX_PAIRS, S, LANES), BF16), pltpu.VMEM((FOX_PAIRS, S, LANES), BF16)] + ex_scratch,
        compiler_params=_cparams(),
    )(*([proj] * (3 * FOX_PAIRS)), gates, w2, *ex_in)
    return res[0], res[1], res[2], res[3:]


def _fox_norm_bwd(o, dmix, w2, exchange=None):
    ex_in, ex_in_specs, ex_out_specs, ex_out_shape, ex_scratch = _hosted(exchange)

    def body(*refs):
        o_ref, g_ref, w_ref = refs[:3]
        do_ref, dl_ref, dw_ref = refs[3 + len(ex_in):6 + len(ex_in)]
        ex_refs = refs[3:3 + len(ex_in)] + refs[6 + len(ex_in):]
        hp, qi = pl.program_id(0), pl.program_id(1)

        if exchange is not None:
            @pl.when((hp == 0) & (qi == 0))
            def _():
                exchange.start(*exchange.split(ex_refs))

        masks = [_head_mask(0), _head_mask(1)]
        ov = o_ref[...]
        g = g_ref[...]
        r = _head_rms(ov, masks)
        gw = g * w_ref[...]
        gwo = gw * ov
        mean = [jnp.sum(jnp.where(mk, gwo, 0.0), axis=1, keepdims=True) * (1.0 / FHD) for mk in masks]
        do = r * gw - ov * (r * r * r) * jnp.where(masks[0], mean[0], mean[1])
        do_ref[...] = do.astype(BF16)
        doo = do * ov
        dl_ref[...] = _pair_cols(*[jnp.sum(jnp.where(mk, doo, 0.0), axis=1, keepdims=True) for mk in masks])

        @pl.when((hp == 0) & (qi == 0))
        def _():
            dw_ref[...] = jnp.zeros_like(dw_ref)

        dw_ref[...] += jnp.sum(g * ov * r, axis=0, keepdims=True)

        @pl.when((hp == NPAIR - 1) & (qi == NQ - 1))
        def _():
            dw = dw_ref[...]
            dw_ref[...] = dw + pltpu.roll(dw, FHD, 1)
            if exchange is not None:
                exchange.finish(*exchange.split(ex_refs))

    blk = pl.BlockSpec((BQ, LANES), lambda hp, i: (i, hp))
    vec = pl.BlockSpec((1, LANES), lambda hp, i: (0, 0))
    res = pl.pallas_call(
        body, name="fox_norm_bwd", grid=(NPAIR, NQ), in_specs=[blk, blk, vec] + ex_in_specs,
        out_specs=[blk, pl.BlockSpec((None, BQ, 2), lambda hp, i: (hp, i, 0)), vec] + ex_out_specs,
        out_shape=[jax.ShapeDtypeStruct((S, DFOX), BF16), jax.ShapeDtypeStruct((NPAIR, S, 2), F32),
                   jax.ShapeDtypeStruct((1, LANES), F32)] + ex_out_shape,
        scratch_shapes=ex_scratch, compiler_params=_cparams(),
    )(o, dmix, w2, *ex_in)
    return res[0], res[1], res[2], res[3:]


def _fox_bwd(proj, do, gates, lse, delta, exchange=None):
    ex_in, ex_in_specs, ex_out_specs, ex_out_shape, ex_scratch = _hosted(exchange)

    pg = FOX_PAIRS_BWD
    n_in = 3 * pg + 4
    heads = [(pp, e) for pp in range(pg) for e in range(2)]

    def body(*refs):
        qkv_refs = refs[:3 * pg]
        do_ref, g_ref, lse_ref, dl_ref = refs[3 * pg:n_in]
        dproj_ref, dc_ref = refs[n_in + len(ex_in):n_in + 2 + len(ex_in)]
        qa_ref, dq_ref = refs[n_in + 2 + len(ex_in) + len(ex_out_shape):n_in + 4 + len(ex_in) + len(ex_out_shape)]
        ex_refs = refs[n_in:n_in + len(ex_in)] + refs[n_in + 2 + len(ex_in):n_in + 2 + len(ex_in) + len(ex_out_shape)] + refs[-2:]
        grp, kj = pl.program_id(0), pl.program_id(1)

        def head_index(pp, e):
            return 2 * (pg * grp + pp) + e

        if exchange is not None:
            @pl.when((grp == 0) & (kj == 0))
            def _():
                exchange.start(*exchange.split(ex_refs))

        @pl.when(kj == 0)
        def _():
            gt = g_ref[...]
            for pp in range(pg):
                qv = qkv_refs[3 * pp][...]
                for e in range(2):
                    qa_ref[2 * pp + e] = _fox_operand(qv, e, _lane_col(gt, head_index(pp, e)), True)
            dq_ref[...] = jnp.zeros_like(dq_ref)

        @pl.when((grp == 0) & (kj == 0))
        def _():
            dc_ref[...] = jnp.zeros_like(dc_ref)

        masks = [_head_mask(0), _head_mask(1)]
        krows = pl.ds(pl.multiple_of(kj * BQ, BQ), BQ)
        gk = g_ref[krows, :]
        kas = [_fox_operand(qkv_refs[3 * pp + 1][...], e, _lane_col(gk, head_index(pp, e)), False) for pp, e in heads]
        vbs = [qkv_refs[3 * pp + 2][...].astype(BF16) for pp in range(pg)]
        lane = lax.broadcasted_iota(jnp.int32, (BQ, LANES), 1)
        n = range(len(heads))

        def block(qi, carry, diagonal):
            dks, dvs, css = carry
            rows = pl.ds(pl.multiple_of(qi * BQ, BQ), BQ)
            qa = [qa_ref[i, rows, :] for i in n]
            s = [_dot(qa[i], kas[i], 1, 1) for i in n]
            if diagonal:
                s = [jnp.where(_causal_block(), s[i], -jnp.inf) for i in n]
            dov = [do_ref[rows, pp * LANES:(pp + 1) * LANES] for pp in range(pg)]
            doe = [jnp.where(masks[e], dov[pp], jnp.zeros_like(dov[pp])) for pp, e in heads]
            lse2 = [lse_ref[pp, rows, :] for pp in range(pg)]
            dl2 = [dl_ref[pp, rows, :] for pp in range(pg)]
            p = [jnp.exp(s[i] - _lane_col(lse2[heads[i][0]], heads[i][1])) for i in n]
            dp = [_dot(doe[i], vbs[heads[i][0]], 1, 1) for i in n]
            ds = [p[i] * (dp[i] - _lane_col(dl2[heads[i][0]], heads[i][1])) for i in n]
            dv_part = [_dot(p[i], doe[i], 0, 0) for i in n]
            dk_part = [_dot(ds[i], jnp.where(masks[heads[i][1]], qa[i], jnp.zeros_like(qa[i])), 0, 0) for i in n]
            dq_part = [jnp.where(masks[heads[i][1]], _dot(ds[i], kas[i]), 0.0) for i in n]
            css = tuple(css[i] + jnp.sum(ds[i], axis=0, keepdims=True) for i in n)
            dc = jnp.zeros((BQ, LANES), F32)
            for i in n:
                dc = dc + jnp.where(lane == head_index(*heads[i]), jnp.sum(ds[i], axis=1, keepdims=True), 0.0)
            for pp in range(pg):
                dq_ref[pp, rows, :] += (dq_part[2 * pp] + dq_part[2 * pp + 1]) * FOX_SCALE
            dc_ref[rows, :] += dc
            dks = tuple(dks[pp] + dk_part[2 * pp] + dk_part[2 * pp + 1] for pp in range(pg))
            dvs = tuple(dvs[pp] + dv_part[2 * pp] + dv_part[2 * pp + 1] for pp in range(pg))
            return dks, dvs, css

        zero = jnp.zeros((BQ, LANES), F32)
        first = block(kj, ((zero,) * pg, (zero,) * pg, (jnp.zeros((1, BQ), F32),) * len(heads)), True)
        dks, dvs, css = lax.fori_loop(kj + 1, NQ, lambda qi, carry: block(qi, carry, False), first)
        r = lax.broadcasted_iota(jnp.int32, (BQ, BQ), 0)
        c = lax.broadcasted_iota(jnp.int32, (BQ, BQ), 1)
        dcol = jnp.zeros((BQ, LANES), F32)
        for i in n:
            col = jnp.sum(jnp.where(r == c, css[i], 0.0), axis=1, keepdims=True)
            dcol = dcol + jnp.where(lane == head_index(*heads[i]), col, 0.0)
        dc_ref[krows, :] -= dcol
        for pp in range(pg):
            base = 3 * pp * LANES
            dproj_ref[krows, base + LANES:base + 2 * LANES] = dks[pp].astype(BF16)
            dproj_ref[krows, base + 2 * LANES:base + 3 * LANES] = dvs[pp].astype(BF16)

        @pl.when(kj == NQ - 1)
        def _():
            for pp in range(pg):
                dproj_ref[:, 3 * pp * LANES:(3 * pp + 1) * LANES] = dq_ref[pp].astype(BF16)

        if exchange is not None:
            @pl.when((grp == NPAIR // pg // 2) & (kj == 0))
            def _():
                exchange.middle(*exchange.split(ex_refs))

            @pl.when((grp == NPAIR // pg - 1) & (kj == NQ - 1))
            def _():
                exchange.rest(*exchange.split(ex_refs))

    qkv_specs = []
    for pp in range(pg):
        qkv_specs.append(pl.BlockSpec((S, LANES), lambda g, j, pp=pp: (0, 3 * (pg * g + pp))))
        qkv_specs.append(pl.BlockSpec((BQ, LANES), lambda g, j, pp=pp: (j, 3 * (pg * g + pp) + 1)))
        qkv_specs.append(pl.BlockSpec((BQ, LANES), lambda g, j, pp=pp: (j, 3 * (pg * g + pp) + 2)))
    pair = pl.BlockSpec((pg, S, 2), lambda g, j: (g, 0, 0))
    res = pl.pallas_call(
        body, name="fox_bwd", grid=(NPAIR // pg, NQ),
        in_specs=qkv_specs + [pl.BlockSpec((S, pg * LANES), lambda g, j: (0, g)), pl.BlockSpec((S, LANES), lambda g, j: (0, 0)),
                              pair, pair] + ex_in_specs,
        out_specs=[pl.BlockSpec((S, 3 * pg * LANES), lambda g, j: (0, g)), pl.BlockSpec((S, LANES), lambda g, j: (0, 0))]
        + ex_out_specs,
        out_shape=[jax.ShapeDtypeStruct((S, DPROJ_PAD), BF16), jax.ShapeDtypeStruct((S, LANES), F32)] + ex_out_shape,
        scratch_shapes=[pltpu.VMEM((2 * pg, S, LANES), BF16), pltpu.VMEM((pg, S, LANES), F32)] + ex_scratch,
        compiler_params=_cparams(),
    )(*([proj] * (3 * pg)), do, gates, lse, delta, *ex_in)
    return res[0], res[1], res[2:]


NQKV = 3 * NGH
GDN_QSCALE = GHD ** -0.5


def _shift_down(x, s):
    if s == 0:
        return x
    row = lax.broadcasted_iota(jnp.int32, x.shape, 0)
    return jnp.where(row >= s, pltpu.roll(x, s, 0), 0.0)


def _shift_up(x, s):
    if s == 0:
        return x
    n = x.shape[0]
    row = lax.broadcasted_iota(jnp.int32, x.shape, 0)
    return jnp.where(row < n - s, pltpu.roll(x, n - s, 0), 0.0)


def _conv_taps(xv):
    return [_shift_down(xv, CONV_K - 1 - j) for j in range(CONV_K)]


def _conv_pre(taps, wv):
    pre = taps[CONV_K - 1] * wv[CONV_K - 1:CONV_K, :]
    for j in range(CONV_K - 1):
        pre = pre + taps[j] * wv[j:j + 1, :]
    return pre


def _l2_factors(b):
    return b < 2 * NGH, jnp.where(b < NGH, GDN_QSCALE, 1.0)


def _gdn_pre(proj, conv_w):
    def body(x_ref, w_ref, o_ref):
        b = pl.program_id(0)
        c = _silu(_conv_pre(_conv_taps(x_ref[...]), w_ref[...]))
        normed, scale = _l2_factors(b)
        rs = lax.rsqrt(jnp.sum(c * c, axis=-1, keepdims=True) + EPS)
        o_ref[...] = c * jnp.where(normed, rs, 1.0) * scale

    return pl.pallas_call(
        body, name="gdn_pre", grid=(NQKV,),
        in_specs=[pl.BlockSpec((S, GHD), lambda b: (0, BLK_GDN + b)), pl.BlockSpec((CONV_K, GHD), lambda b: (0, b))],
        out_specs=pl.BlockSpec((S, GHD), lambda b: (0, b)),
        out_shape=jax.ShapeDtypeStruct((S, NQKV * GHD), F32), compiler_params=_cparams(),
    )(proj, conv_w)


def _gdn_pre_bwd(proj, conv_w, dqkv, dproj):
    def body(x_ref, w_ref, dy_ref, dproj_in, dx_ref, dw_ref):
        del dproj_in
        b = pl.program_id(0)
        taps = _conv_taps(x_ref[...])
        wv = w_ref[...]
        pre = _conv_pre(taps, wv)
        sig = _sigmoid(pre)
        c = pre * sig
        normed, scale = _l2_factors(b)
        g = dy_ref[...] * scale
        rs = lax.rsqrt(jnp.sum(c * c, axis=-1, keepdims=True) + EPS)
        dc_n = rs * g - c * (rs * rs * rs) * jnp.sum(g * c, axis=-1, keepdims=True)
        dc = jnp.where(normed, dc_n, g)
        dpre = dc * sig * (1.0 + pre * (1.0 - sig))
        dx = dpre * wv[CONV_K - 1:CONV_K, :]
        for j in range(CONV_K - 1):
            dx = dx + _shift_up(dpre, CONV_K - 1 - j) * wv[j:j + 1, :]
        dx_ref[...] = dx.astype(BF16)
        for j in range(CONV_K):
            dw_ref[j:j + 1, :] = jnp.sum(dpre * taps[j], axis=0, keepdims=True)

    return pl.pallas_call(
        body, name="gdn_pre_bwd", grid=(NQKV,),
        in_specs=[pl.BlockSpec((S, GHD), lambda b: (0, BLK_GDN + b)), pl.BlockSpec((CONV_K, GHD), lambda b: (0, b)),
                  pl.BlockSpec((None, S, GHD), lambda b: (b // NGH, 0, b % NGH)), pl.BlockSpec(memory_space=pl.ANY)],
        out_specs=[pl.BlockSpec((S, GHD), lambda b: (0, BLK_GDN + b)), pl.BlockSpec((CONV_K, GHD), lambda b: (0, b))],
        out_shape=[jax.ShapeDtypeStruct((S, DPROJ_PAD), BF16), jax.ShapeDtypeStruct((CONV_K, NQKV * GHD), F32)],
        input_output_aliases={3: 0}, compiler_params=_cparams(),
    )(proj, conv_w, dqkv, dproj)


CB = 16
NCB = NCH // CB


def _chunk_prep(qs, ks, vs, gcols, bcols, t_saved=None):
    n = range(len(qs))
    r = lax.broadcasted_iota(jnp.int32, (CHUNK, CHUNK), 0)
    c = lax.broadcasted_iota(jnp.int32, (CHUNK, CHUNK), 1)
    incl = c <= r
    eye = (r == c).astype(F32)
    grow = [jnp.sum(gcols[i] * eye, axis=0, keepdims=True) for i in n]
    gc_col = [jnp.sum(jnp.where(incl, grow[i], 0.0), axis=1, keepdims=True) for i in n]
    gc_row = [jnp.sum(jnp.where(r <= c, gcols[i], 0.0), axis=0, keepdims=True) for i in n]
    decay = [jnp.exp(jnp.where(incl, gc_col[i] - gc_row[i], -jnp.inf)) for i in n]
    kb = [ks[i] * bcols[i] for i in n]
    vb = [vs[i] * bcols[i] for i in n]
    kk = [_mm_nt(kb[i], ks[i]) for i in n]
    m = [jnp.where(c < r, kk[i] * decay[i], 0.0) for i in n]
    if t_saved is None:
        t_inv = [eye - m[i] for i in n]
        p = [_dot3(m[i], m[i]) for i in n]
        for step in range(5):
            t_inv = [t_inv[i] + _dot3(t_inv[i], p[i]) for i in n]
            if step < 4:
                p = [_dot3(p[i], p[i]) for i in n]
    else:
        t_inv = [_saved_inverse(m[i], t_saved[i]) for i in n]
    egc = [jnp.exp(gc_col[i]) for i in n]
    u = [_mm_nn(t_inv[i], vb[i]) for i in n]
    w = [_mm_nn(t_inv[i], kb[i] * egc[i]) for i in n]
    qk = [_mm_nt(qs[i], ks[i]) for i in n]
    gc_last = [gc_col[i][CHUNK - 1:CHUNK, :] for i in n]
    return [(u[i], w[i], qk[i] * decay[i], qs[i] * egc[i], ks[i] * jnp.exp(gc_last[i] - gc_col[i]), jnp.exp(gc_last[i]),
             t_inv[i]) for i in n]


def _prep_specs():
    rows = CB * CHUNK
    qs = pl.BlockSpec((rows, GHD), lambda i, h: (i, h))
    ks = pl.BlockSpec((rows, GHD), lambda i, h: (i, NGH + h))
    vs = pl.BlockSpec((rows, GHD), lambda i, h: (i, 2 * NGH + h))
    gs = pl.BlockSpec((rows, LANES), lambda i, h: (i, 0))
    a_s = pl.BlockSpec((None, rows, CHUNK), lambda i, h: (h, i, 0))
    gl_s = pl.BlockSpec((None, CB, 1, LANES), lambda i, h: (h, i, 0, 0))
    return qs, ks, vs, gs, a_s, gl_s


def _gdn_prep(qkv, gates, exchange=None):
    ex_in, ex_in_specs, ex_out_specs, ex_out_shape, ex_scratch = _hosted(exchange)

    def body(*refs):
        q_ref, k_ref, v_ref, g_ref = refs[:4]
        u_ref, w_ref, qd_ref, kd_ref, a_ref, gl_ref, t_ref = refs[4 + len(ex_in):11 + len(ex_in)]
        ex_refs = refs[4:4 + len(ex_in)] + refs[11 + len(ex_in):]
        h = pl.program_id(1)

        if exchange is not None:
            @pl.when((pl.program_id(0) == 0) & (h == 0))
            def _():
                exchange.start(*exchange.split(ex_refs))

        chunks = [pl.ds(cidx * CHUNK, CHUNK) for cidx in range(CB)]
        gts = [g_ref[rows, :] for rows in chunks]
        outs = _chunk_prep([q_ref[rows, :] for rows in chunks], [k_ref[rows, :] for rows in chunks],
                           [v_ref[rows, :] for rows in chunks], [_lane_col(gt, LANE_G + h) for gt in gts],
                           [_lane_col(gt, LANE_BETA + h) for gt in gts])
        for cidx, rows in enumerate(chunks):
            u, w, a, qd, kd, gl, t_inv = outs[cidx]
            u_ref[rows, :] = u
            w_ref[rows, :] = w
            qd_ref[rows, :] = qd
            kd_ref[rows, :] = kd
            a_ref[rows, :] = a
            t_ref[rows, :] = t_inv
            gl_ref[cidx] = jnp.broadcast_to(gl, (1, LANES))

        if exchange is not None:
            @pl.when((pl.program_id(0) == NCB // 2) & (h == 0))
            def _():
                exchange.middle(*exchange.split(ex_refs))

            @pl.when((pl.program_id(0) == NCB - 1) & (h == NGH - 1))
            def _():
                exchange.rest(*exchange.split(ex_refs))

    qs, ks, vs, gs, a_s, gl_s = _prep_specs()
    tok = jax.ShapeDtypeStruct((S, DGDN), F32)
    sq = jax.ShapeDtypeStruct((NGH, S, CHUNK), F32)
    res = pl.pallas_call(
        body, name="gdn_prep", grid=(NCB, NGH), in_specs=[qs, ks, vs, gs] + ex_in_specs,
        out_specs=[qs, qs, qs, qs, a_s, gl_s, a_s] + ex_out_specs,
        out_shape=[tok, tok, tok, tok, sq, jax.ShapeDtypeStruct((NGH, NCH, 1, LANES), F32), sq] + ex_out_shape,
        scratch_shapes=ex_scratch, compiler_params=_cparams(),
    )(qkv, qkv, qkv, gates, *ex_in)
    return res[:7], res[7:]


def _gdn_prep_bwd(qkv, gates, t_inv, du, dw, dqd, dkd, da, dgl, exchange=None):
    ex_in, ex_in_specs, ex_out_specs, ex_out_shape, ex_scratch = _hosted(exchange)

    def body(*refs):
        q_ref, k_ref, v_ref, g_ref, t_ref, du_ref, dw_ref, dqd_ref, dkd_ref, da_ref, dgl_ref = refs[:11]
        dqkv_ref, dg_ref = refs[11 + len(ex_in):13 + len(ex_in)]
        ex_refs = refs[11:11 + len(ex_in)] + refs[13 + len(ex_in):]
        h = pl.program_id(1)

        if exchange is not None:
            @pl.when((pl.program_id(0) == 0) & (h == 0))
            def _():
                exchange.start(*exchange.split(ex_refs))

        @pl.when(h == 0)
        def _():
            dg_ref[...] = jnp.zeros_like(dg_ref)

        lane = lax.broadcasted_iota(jnp.int32, (CHUNK, LANES), 1)
        chunks = [pl.ds(cidx * CHUNK, CHUNK) for cidx in range(CB)]
        gts = [g_ref[rows, :] for rows in chunks]
        t_saved = [t_ref[rows, :] for rows in chunks]
        _, vjp = jax.vjp(lambda *args: [o[:6] for o in _chunk_prep(*args, t_saved=t_saved)],
                         [q_ref[rows, :] for rows in chunks], [k_ref[rows, :] for rows in chunks],
                         [v_ref[rows, :] for rows in chunks], [_lane_col(gt, LANE_G + h) for gt in gts],
                         [_lane_col(gt, LANE_BETA + h) for gt in gts])
        dqs, dks, dvs, dgcs, dbcs = vjp([(du_ref[rows, :], dw_ref[rows, :], da_ref[rows, :], dqd_ref[rows, :],
                                          dkd_ref[rows, :], dgl_ref[cidx][:, 0:1]) for cidx, rows in enumerate(chunks)])
        for cidx, rows in enumerate(chunks):
            dq, dk, dv, dgc, dbc = dqs[cidx], dks[cidx], dvs[cidx], dgcs[cidx], dbcs[cidx]
            dqkv_ref[0, rows, :] = dq
            dqkv_ref[1, rows, :] = dk
            dqkv_ref[2, rows, :] = dv
            dg_ref[rows, :] += jnp.where(lane == LANE_G + h, dgc, 0.0) + jnp.where(lane == LANE_BETA + h, dbc, 0.0)

        if exchange is not None:
            @pl.when((pl.program_id(0) == NCB - 1) & (h == NGH - 1))
            def _():
                exchange.finish(*exchange.split(ex_refs))

    qs, ks, vs, gs, a_s, gl_s = _prep_specs()
    res = pl.pallas_call(
        body, name="gdn_prep_bwd", grid=(NCB, NGH), in_specs=[qs, ks, vs, gs, a_s, qs, qs, qs, qs, a_s, gl_s] + ex_in_specs,
        out_specs=[pl.BlockSpec((3, CB * CHUNK, GHD), lambda i, h: (0, i, h)), gs] + ex_out_specs,
        out_shape=[jax.ShapeDtypeStruct((3, S, DGDN), F32), jax.ShapeDtypeStruct((S, LANES), F32)] + ex_out_shape,
        scratch_shapes=ex_scratch, compiler_params=_cparams(),
    )(qkv, qkv, qkv, gates, t_inv, du, dw, dqd, dkd, da, dgl, *ex_in)
    return res[0], res[1], res[2:]


def _scan_specs(nh, parts, reverse):
    wide, rows, chunks = nh * GHD, S // parts, NCH // parts

    def part(p):
        return parts - 1 - p if reverse else p

    hs = pl.BlockSpec((rows, wide), lambda g, p: (part(p), g))
    a_s = pl.BlockSpec((nh, rows, CHUNK), lambda g, p: (g, part(p), 0))
    gl_s = pl.BlockSpec((nh, chunks, 1, LANES), lambda g, p: (g, part(p), 0, 0))
    st_s = pl.BlockSpec((nh, chunks, GHD, GHD), lambda g, p: (g, part(p), 0, 0))
    gz_s = pl.BlockSpec((rows, wide), lambda g, p: (part(p), BLK_GZ // nh + g))
    mix_s = pl.BlockSpec((rows, wide), lambda g, p: (part(p), NPAIR // nh + g))
    return hs, a_s, gl_s, st_s, gz_s, mix_s


def _head_cols(hh):
    return slice(hh * GHD, (hh + 1) * GHD)


SCAN_HEADS, SCAN_PARTS = 4, 2
SCAN_HEADS_BWD, SCAN_PARTS_BWD = 4, 4


def _gdn_scan(u, w, qd, kd, a, gl, proj, w_norm, mix):
    heads = range(SCAN_HEADS)

    def body(u_ref, w_ref, qd_ref, kd_ref, a_ref, gl_ref, z_ref, wn_ref, mix_in, mix_ref, o_ref, st_ref, carry_ref):
        del mix_in

        @pl.when(pl.program_id(1) == 0)
        def _():
            carry_ref[...] = jnp.zeros_like(carry_ref)

        def step(ci, states):
            rows = pl.ds(pl.multiple_of(ci * CHUNK, CHUNK), CHUNK)
            for hh in heads:
                st_ref[hh, ci] = states[hh]
            ws = [_dot(w_ref[rows, _head_cols(hh)], states[hh]) for hh in heads]
            qs = [_dot(qd_ref[rows, _head_cols(hh)], states[hh]) for hh in heads]
            vn = [u_ref[rows, _head_cols(hh)] - ws[hh] for hh in heads]
            av = [_dot(a_ref[hh, rows, :], vn[hh]) for hh in heads]
            kv = [_dot(kd_ref[rows, _head_cols(hh)], vn[hh], 0, 0) for hh in heads]
            for hh in heads:
                o_ref[rows, _head_cols(hh)] = qs[hh] + av[hh]
            return tuple(states[hh] * gl_ref[hh, ci] + kv[hh] for hh in heads)

        last = lax.fori_loop(0, NCH // SCAN_PARTS, step, tuple(carry_ref[hh] for hh in heads))
        for hh in heads:
            carry_ref[hh] = last[hh]
            ov = o_ref[:, _head_cols(hh)]
            mix_ref[:, _head_cols(hh)] = (ov * _rms_scale(ov) * wn_ref[...] * _silu(z_ref[:, _head_cols(hh)])).astype(BF16)

    hs, a_s, gl_s, st_s, gz_s, mix_s = _scan_specs(SCAN_HEADS, SCAN_PARTS, False)
    return pl.pallas_call(
        body, name="gdn_scan", grid=(NGH // SCAN_HEADS, SCAN_PARTS),
        in_specs=[hs, hs, hs, hs, a_s, gl_s, gz_s, pl.BlockSpec((1, GHD), lambda g, p: (0, 0)),
                  pl.BlockSpec(memory_space=pl.ANY)],
        out_specs=[mix_s, hs, st_s],
        out_shape=[jax.ShapeDtypeStruct((S, D), BF16), jax.ShapeDtypeStruct((S, DGDN), F32),
                   jax.ShapeDtypeStruct((NGH, NCH, GHD, GHD), F32)],
        input_output_aliases={8: 0}, scratch_shapes=[pltpu.VMEM((SCAN_HEADS, GHD, GHD), F32)], compiler_params=_cparams(),
    )(u, w, qd, kd, a, gl, proj, w_norm, mix)


def _gdn_scan_bwd(dmix, o, proj, w_norm, u, w, qd, kd, a, gl, states, dproj, exchange=None):
    ex_in, ex_in_specs, ex_out_specs, ex_out_shape, ex_scratch = _hosted(exchange)
    groups = NGH // SCAN_HEADS_BWD

    def body(*refs):
        dy_ref, o_ref, z_ref, wn_ref, u_ref, w_ref, qd_ref, kd_ref, a_ref, gl_ref, st_ref = refs[:11]
        dz_ref, du_ref, dw_ref, dqd_ref, dkd_ref, da_ref, dgl_ref, dwn_ref = refs[12 + len(ex_in):20 + len(ex_in)]
        do_ref, carry_ref = refs[20 + len(ex_in) + len(ex_out_shape):22 + len(ex_in) + len(ex_out_shape)]
        ex_refs = refs[12:12 + len(ex_in)] + refs[20 + len(ex_in):20 + len(ex_in) + len(ex_out_shape)] + refs[-2:]
        heads = range(SCAN_HEADS_BWD)
        chunks = NCH // SCAN_PARTS_BWD

        if exchange is not None:
            @pl.when((pl.program_id(0) == 0) & (pl.program_id(1) == 0))
            def _():
                exchange.start(*exchange.split(ex_refs))

        @pl.when((pl.program_id(0) == 0) & (pl.program_id(1) == 0))
        def _():
            dwn_ref[...] = jnp.zeros_like(dwn_ref)

        @pl.when(pl.program_id(1) == 0)
        def _():
            carry_ref[...] = jnp.zeros_like(carry_ref)

        wn = wn_ref[...]
        for hh in heads:
            c = _head_cols(hh)
            ov = o_ref[:, c]
            zv = z_ref[:, c]
            g = dy_ref[:, c]
            sig = _sigmoid(zv)
            dz_ref[:, c] = (g * (ov * _rms_scale(ov) * wn) * sig * (1.0 + zv * (1.0 - sig))).astype(BF16)
            do, dwt = _rms_bwd(ov, wn, g * zv * sig)
            do_ref[:, c] = do
            dwn_ref[...] += jnp.sum(dwt, axis=0, keepdims=True)

        def step(t, dstates):
            ci = chunks - 1 - t
            rows = pl.ds(pl.multiple_of(ci * CHUNK, CHUNK), CHUNK)
            cols = [_head_cols(hh) for hh in heads]
            state = [st_ref[hh, ci] for hh in heads]
            dov = [do_ref[rows, cols[hh]] for hh in heads]
            wv = [w_ref[rows, cols[hh]] for hh in heads]
            ws = [_dot(wv[hh], state[hh]) for hh in heads]
            adov = [_dot(a_ref[hh, rows, :], dov[hh], 0, 0) for hh in heads]
            kds = [_dot(kd_ref[rows, cols[hh]], dstates[hh]) for hh in heads]
            dqd = [_dot(dov[hh], state[hh], 1, 1) for hh in heads]
            qdo = [_dot(qd_ref[rows, cols[hh]], dov[hh], 0, 0) for hh in heads]
            vn = [u_ref[rows, cols[hh]] - ws[hh] for hh in heads]
            dvn = [adov[hh] + kds[hh] for hh in heads]
            da = [_dot(dov[hh], vn[hh], 1, 1) for hh in heads]
            dkd = [_dot(vn[hh], dstates[hh], 1, 1) for hh in heads]
            dwv = [_dot(dvn[hh], state[hh], 1, 1) for hh in heads]
            wdv = [_dot(wv[hh], dvn[hh], 0, 0) for hh in heads]
            for hh in heads:
                da_ref[hh, rows, :] = da[hh]
                dqd_ref[rows, cols[hh]] = dqd[hh]
                dkd_ref[rows, cols[hh]] = dkd[hh]
                dgl = jnp.sum(jnp.sum(dstates[hh] * state[hh], axis=1, keepdims=True), axis=0, keepdims=True)
                dgl_ref[hh, ci] = jnp.broadcast_to(dgl, (1, LANES))
                du_ref[rows, cols[hh]] = dvn[hh]
                dw_ref[rows, cols[hh]] = -dwv[hh]
            return tuple(dstates[hh] * gl_ref[hh, ci] + qdo[hh] - wdv[hh] for hh in heads)

        last = lax.fori_loop(0, chunks, step, tuple(carry_ref[hh] for hh in heads))
        for hh in heads:
            carry_ref[hh] = last[hh]

        if exchange is not None:
            @pl.when((pl.program_id(0) == groups - 1) & (pl.program_id(1) == SCAN_PARTS_BWD - 1))
            def _():
                exchange.finish(*exchange.split(ex_refs))

    hs, a_s, gl_s, st_s, gz_s, mix_s = _scan_specs(SCAN_HEADS_BWD, SCAN_PARTS_BWD, True)
    vec = pl.BlockSpec((1, GHD), lambda g, p: (0, 0))
    tok = jax.ShapeDtypeStruct((S, DGDN), F32)
    res = pl.pallas_call(
        body, name="gdn_scan_bwd", grid=(groups, SCAN_PARTS_BWD),
        in_specs=[mix_s, hs, gz_s, vec, hs, hs, hs, hs, a_s, gl_s, st_s, pl.BlockSpec(memory_space=pl.ANY)] + ex_in_specs,
        out_specs=[gz_s, hs, hs, hs, hs, a_s, gl_s, vec] + ex_out_specs,
        out_shape=[jax.ShapeDtypeStruct((S, DPROJ_PAD), BF16), tok, tok, tok, tok,
                   jax.ShapeDtypeStruct((NGH, S, CHUNK), F32), jax.ShapeDtypeStruct((NGH, NCH, 1, LANES), F32),
                   jax.ShapeDtypeStruct((1, GHD), F32)] + ex_out_shape,
        input_output_aliases={11: 0},
        scratch_shapes=[pltpu.VMEM((S // SCAN_PARTS_BWD, SCAN_HEADS_BWD * GHD), F32),
                        pltpu.VMEM((SCAN_HEADS_BWD, GHD, GHD), F32)] + ex_scratch,
        compiler_params=_cparams(),
    )(dmix, o, proj, w_norm, u, w, qd, kd, a, gl, states, dproj, *ex_in)
    return res[:8], res[8:]


def _place():
    return lax.axis_index("x"), lax.axis_index("y"), lax.axis_index("c")


def _place_scalars():
    x, y, c = _place()
    return jnp.stack([2 * x + y, c]).astype(jnp.int32)


def _other_chips(x, y):
    return [(1 - x, y), (x, 1 - y), (1 - x, 1 - y)]


HBM = pl.BlockSpec(memory_space=pltpu.HBM)
VMEM = pl.BlockSpec(memory_space=pltpu.VMEM)


def _half_rows(ref_or_rows, half):
    rows = ref_or_rows // 2
    return pl.ds(pl.multiple_of(half * rows, rows), rows)


class _Exchange:
    def __init__(self, inputs, out_shape, n_sems, start, finish=None, middle=None, rest=None):
        self.inputs, self.out_shape, self.n_sems, self.start = inputs, out_shape, n_sems, start
        if finish is None:
            def finish(*refs):
                middle(*refs)
                rest(*refs)
        self.finish = finish
        self.middle = middle if middle is not None else (lambda *refs: None)
        self.rest = rest if rest is not None else finish

    def sem_shapes(self):
        return [pltpu.SemaphoreType.DMA((self.n_sems,)), pltpu.SemaphoreType.DMA((self.n_sems,))]

    def split(self, refs):
        n_in, n_out = len(self.inputs), len(self.out_shape)
        return refs[:n_in], refs[n_in:n_in + n_out], refs[n_in + n_out], refs[n_in + n_out + 1]


def _allgather_exchange(shards, whole=()):
    n, nw = len(shards), len(whole)
    slots = 8

    def plan(src, outs, send_sems, recv_sems):
        x, y, c = _place()
        via_x, via_y, diagonal = _other_chips(x, y)
        id_x, id_y, id_diagonal = [2 * chip[0] + chip[1] for chip in (via_x, via_y, diagonal)]
        me, sibling = (x, y, c), (x, y, 1 - c)

        def rows_of(a, half, quarter):
            total = src[a].shape[0]
            if quarter is None:
                return _half_rows(total, half)
            return pl.ds(pl.multiple_of(half * (total // 2) + quarter * (total // 4), total // 4), total // 4)

        def copy(a, k, chip_index, half, quarter, to, from_src=False):
            rows = rows_of(a, half, quarter)
            dst = outs[a].at[chip_index, rows]
            return pltpu.make_async_remote_copy(
                src_ref=src[a].at[rows] if from_src else dst, dst_ref=dst, send_sem=send_sems.at[slots * a + k],
                recv_sem=recv_sems.at[slots * a + k], device_id=to, device_id_type=MESH)

        def whole_copy(b, k, chip_index, to):
            return pltpu.make_async_remote_copy(
                src_ref=src[n + b], dst_ref=outs[n + b].at[chip_index], send_sem=send_sems.at[slots * n + 3 * b + k],
                recv_sem=recv_sems.at[slots * n + 3 * b + k], device_id=to, device_id_type=MESH)

        first, stages, last = [], [], []
        for a in range(n):
            first += [copy(a, 0, 2 * x + y, c, None, (*via_x, c), True), copy(a, 1, 2 * x + y, c, None, (*via_y, c), True)]
            stages.append([
                (copy(a, 0, id_x, c, None, me),
                 [copy(a, 2, id_x, c, 0, (*via_y, c)), copy(a, 4, id_x, c, None, sibling)]),
                (copy(a, 1, id_y, c, None, me),
                 [copy(a, 3, id_y, c, 1, (*via_x, c)), copy(a, 5, id_y, c, None, sibling)]),
                (copy(a, 2, id_diagonal, c, 0, me), [copy(a, 6, id_diagonal, c, 0, sibling)]),
                (copy(a, 3, id_diagonal, c, 1, me), [copy(a, 7, id_diagonal, c, 1, sibling)]),
            ])
            last += [copy(a, 4, id_x, 1 - c, None, me), copy(a, 5, id_y, 1 - c, None, me),
                     copy(a, 6, id_diagonal, 1 - c, 0, me), copy(a, 7, id_diagonal, 1 - c, 1, me)]
        for b in range(nw):
            for k, (chip, index) in enumerate(((via_x, id_x), (via_y, id_y), (diagonal, id_diagonal))):
                first.append(whole_copy(b, k, 2 * x + y, (*chip, c)))
                last.append(whole_copy(b, k, index, me))
        return first, stages, last

    def start(*refs):
        for cp in plan(*refs)[0]:
            cp.start()

    def pass_on(stages, which):
        for stage in which:
            for per_shard in stages:
                lands, onward = per_shard[stage]
                lands.wait_recv()
                for cp in onward:
                    cp.start()

    def middle(*refs):
        pass_on(plan(*refs)[1], (0, 1))

    def rest(*refs):
        first, stages, last = plan(*refs)
        pass_on(stages, (2, 3))
        for cp in last:
            cp.wait_recv()
        for cp in first + [cp for per_shard in stages for _, onward in per_shard for cp in onward]:
            cp.wait_send()

    out_shape = [jax.ShapeDtypeStruct((NCHIP,) + s.shape, s.dtype) for s in list(shards) + list(whole)]
    return _Exchange(list(shards) + list(whole), out_shape, slots * n + 3 * nw, start, middle=middle, rest=rest)


def _with_own(gathered, own):
    x, y, _ = _place()
    return lax.dynamic_update_index_in_dim(gathered, own, 2 * x + y, axis=0)


def _simple_exchange(inputs, out_shape, copies_of):
    def start(*refs):
        for cp in copies_of(*refs):
            cp.start()

    def finish(*refs):
        for cp in copies_of(*refs):
            cp.wait()

    return _Exchange(list(inputs), out_shape, len(out_shape) * 3, start, finish)


def _pair_exchange(grads):
    def copies_of(src, outs, send_sems, recv_sems):
        x, y, c = _place()
        return [pltpu.make_async_remote_copy(
            src_ref=src[a].at[:, _half_rows(src[a].shape[1], 1 - c)], dst_ref=outs[a], send_sem=send_sems.at[a],
            recv_sem=recv_sems.at[a], device_id=(x, y, 1 - c), device_id_type=MESH) for a in range(len(src))]

    return _simple_exchange(
        grads, [jax.ShapeDtypeStruct((g.shape[0], g.shape[1] // 2, g.shape[2]), g.dtype) for g in grads], copies_of)


def _pair_sum(grads, theirs, name):
    n = len(grads)

    def body(place_ref, *refs):
        for a in range(n):
            refs[2 * n + a][...] = (refs[a][...].astype(F32) + refs[n + a][...].astype(F32)).astype(BF16)

    def specs(arrs):
        return [pl.BlockSpec((None,) + g.shape[1:], lambda j, place: (j, 0, 0)) for g in arrs]

    own_half = [pl.BlockSpec((None, g.shape[1] // 2, g.shape[2]), lambda j, place: (j, place[1], 0)) for g in grads]
    return pl.pallas_call(
        body, name=name, grid_spec=pltpu.PrefetchScalarGridSpec(
            num_scalar_prefetch=1, grid=(NCHIP,), in_specs=own_half + specs(theirs), out_specs=specs(theirs)),
        out_shape=[jax.ShapeDtypeStruct(g.shape, BF16) for g in theirs], compiler_params=_cparams(),
    )(_place_scalars(), *grads, *theirs)


def _chip_exchange(parts):
    def copies_of(src, outs, send_sems, recv_sems):
        x, y, c = _place()
        return [pltpu.make_async_remote_copy(
            src_ref=src[a].at[2 * chip[0] + chip[1]], dst_ref=outs[a].at[k], send_sem=send_sems.at[3 * a + k],
            recv_sem=recv_sems.at[3 * a + k], device_id=(*chip, c), device_id_type=MESH)
            for a in range(len(src)) for k, chip in enumerate(_other_chips(x, y))]

    return _simple_exchange(parts, [jax.ShapeDtypeStruct((NCHIP - 1,) + p.shape[1:], p.dtype) for p in parts], copies_of)


def _chip_sum(parts, received, exchange=None):
    n = len(parts)
    steps = 4
    ex_in, ex_in_specs, ex_out_specs, ex_out_shape, ex_scratch = _hosted(exchange)
    n_ex = len(ex_in)

    def body(place_ref, *refs):
        mine, theirs = refs[2 * n + n_ex:3 * n + n_ex], refs[3 * n + n_ex:4 * n + n_ex]
        ex_refs = refs[2 * n:2 * n + n_ex] + refs[4 * n + n_ex:len(refs) - n - 2]
        tiles, send_sems, recv_sems = refs[len(refs) - n - 2:len(refs) - 2], refs[-2], refs[-1]
        step = pl.program_id(0)
        if exchange is not None:
            @pl.when(step == 0)
            def _():
                exchange.start(*exchange.split(ex_refs))

        x, y, c = _place()

        def share(a, i):
            rows = tiles[a].shape[1]
            return pltpu.make_async_remote_copy(
                src_ref=tiles[a].at[i], dst_ref=theirs[a].at[pl.ds(pl.multiple_of(i * rows, rows), rows)],
                send_sem=send_sems.at[a * steps + i], recv_sem=recv_sems.at[a * steps + i], device_id=(x, y, 1 - c),
                device_id_type=MESH)

        for a in range(n):
            own, r = refs[a], refs[n + a]
            total = ((own[...].astype(F32) + r[0].astype(F32)) + r[1].astype(F32)) + r[2].astype(F32)
            mine[a][...] = total
            tiles[a][step] = total
            share(a, step).start()

        @pl.when(step == steps - 1)
        def _():
            if exchange is not None:
                exchange.finish(*exchange.split(ex_refs))
            for a in range(n):
                for i in range(steps):
                    share(a, i).wait()

    own_specs = [pl.BlockSpec((None, g.shape[1] // steps, g.shape[2]), lambda i, place: (place[0], i, 0)) for g in parts]
    received_specs = [pl.BlockSpec((g.shape[0], g.shape[1] // steps, g.shape[2]), lambda i, place: (0, i, 0))
                      for g in received]
    out_specs = [pl.BlockSpec((g.shape[1] // steps, g.shape[2]), lambda i, place: (i, 0)) for g in parts]
    halves = [jax.ShapeDtypeStruct(g.shape[1:], F32) for g in parts]
    res = pl.pallas_call(
        body, name="grads_chip_sum", grid_spec=pltpu.PrefetchScalarGridSpec(
            num_scalar_prefetch=1, grid=(steps,), in_specs=own_specs + received_specs + ex_in_specs,
            out_specs=out_specs + [HBM] * n + ex_out_specs,
            scratch_shapes=ex_scratch + [pltpu.VMEM((steps, g.shape[1] // steps, g.shape[2]), F32) for g in parts]
            + [pltpu.SemaphoreType.DMA((n * steps,))] * 2),
        out_shape=halves * 2 + ex_out_shape, compiler_params=_cparams(),
    )(_place_scalars(), *parts, *received, *ex_in)
    return res[:n], res[n:2 * n], res[2 * n:]


def _adamw_math(w, g, m, v):
    nm = ADAM_B1 * m + (1.0 - ADAM_B1) * g
    nv = ADAM_B2 * v + (1.0 - ADAM_B2) * jnp.square(g)
    m_hat = nm / (1.0 - ADAM_B1 ** ADAM_STEP)
    v_hat = nv / (1.0 - ADAM_B2 ** ADAM_STEP)
    return -ADAM_LR * (m_hat / (jnp.sqrt(v_hat) + ADAM_EPS) + ADAM_WD * w), nm, nv


def _adamw_big(ws, g_mine, g_theirs, ms, vs):
    n = len(ws)
    steps = 8
    per_half = steps // 2

    def body(place_ref, *refs):
        outs = refs[5 * n:]
        own_half = (pl.program_id(0) // per_half) == place_ref[1]
        for a in range(n):
            g = jnp.where(own_half, refs[n + a][...], refs[2 * n + a][...])
            d, nm, nv = _adamw_math(refs[a][...], g, refs[3 * n + a][...], refs[4 * n + a][...])
            outs[a][...] = g
            outs[n + a][...] = d
            outs[2 * n + a][...] = nm
            outs[3 * n + a][...] = nv

    specs = [pl.BlockSpec((w.shape[0] // steps, w.shape[1]), lambda i, place: (i, 0)) for w in ws]

    def half_specs(halves, of_this_core):
        def tile(i, place):
            first = (place[1] if of_this_core else 1 - place[1]) * per_half
            return jnp.clip(i - first, 0, per_half - 1), 0

        return [pl.BlockSpec((g.shape[0] // per_half, g.shape[1]), tile) for g in halves]

    shapes = [jax.ShapeDtypeStruct(w.shape, F32) for w in ws]
    res = pl.pallas_call(
        body, name="adamw_big", grid_spec=pltpu.PrefetchScalarGridSpec(
            num_scalar_prefetch=1, grid=(steps,),
            in_specs=specs + half_specs(g_mine, True) + half_specs(g_theirs, False) + specs * 2, out_specs=specs * 4),
        out_shape=shapes * 4, compiler_params=_cparams(),
    )(_place_scalars(), *ws, *g_mine, *g_theirs, *ms, *vs)
    return res[:n], res[n:2 * n], res[2 * n:3 * n], res[3 * n:]


def _adamw_in(w, g_mine, g_theirs, m, v):
    half = D // 2

    def body(w_ref, gm_ref, gt_ref, m_ref, v_ref, g_out, d_out, nm_out, nv_out, g_ref):
        south = lax.axis_index("c") == 0
        g_ref[0:half, :] = jnp.where(south, gm_ref[...], gt_ref[...])
        g_ref[half:D, :] = jnp.where(south, gt_ref[...], gm_ref[...])
        g = g_ref[0:CW, :]
        d, nm, nv = _adamw_math(w_ref[...], g, m_ref[...], v_ref[...])
        g_out[...] = g
        d_out[...] = d
        nm_out[...] = nm
        nv_out[...] = nv

    cols = 2 * LANES
    spec = pl.BlockSpec((CW, cols), lambda i: (0, i))
    half_spec = pl.BlockSpec((half, cols), lambda i: (0, i))
    return pl.pallas_call(
        body, name="adamw_in", grid=(D // cols,), in_specs=[spec, half_spec, half_spec, spec, spec], out_specs=[spec] * 4,
        out_shape=[jax.ShapeDtypeStruct((CW, D), F32)] * 4, scratch_shapes=[pltpu.VMEM((D, cols), F32)],
        compiler_params=_cparams(),
    )(w, g_mine, g_theirs, m, v)


NORM_NAMES = ("pre_mix_norm", "post_mix_norm", "pre_mlp_norm", "post_mlp_norm")
SMALL_NAMES = NORM_NAMES + ("gdn_conv_w", "fox_f_bias", "gdn_dt_bias", "gdn_a_log", "fox_out_norm", "gdn_out_norm")
CONV_COLS = 3 * DGDN // NCHIP


def _small_gather(d_norms, d_conv, sums, d_fox_norm, d_gdn_norm, loss_row):
    n_arrays = 6
    n_remote = n_arrays * (NDEV - 1)

    def copies_of(src, outs, send_sems, recv_sems):
        x, y, c = _place()
        me = 4 * x + 2 * y + c

        def from_me(chip_index):
            cols = pl.ds(pl.multiple_of(chip_index * CONV_COLS, LANES), CONV_COLS)
            return [src[0], src[1].at[:, cols], src[2], src[3], src[4], src[5]]

        local = [pltpu.make_async_copy(s, outs[a].at[me], send_sems.at[n_remote + a]) for a, s in enumerate(from_me(2 * x + y))]
        remote = []
        for k in range(1, NDEV):
            px, py, pc = x ^ ((k >> 2) & 1), y ^ ((k >> 1) & 1), c ^ (k & 1)
            remote += [pltpu.make_async_remote_copy(
                src_ref=s, dst_ref=outs[a].at[me], send_sem=send_sems.at[n_arrays * (k - 1) + a],
                recv_sem=recv_sems.at[n_arrays * (k - 1) + a], device_id=(px, py, pc), device_id_type=MESH)
                for a, s in enumerate(from_me(2 * px + py))]
        return local + remote

    def start(*refs):
        for cp in copies_of(*refs):
            cp.start()

    def finish(*refs):
        for cp in copies_of(*refs):
            cp.wait()

    shapes = [(4, D), (CONV_K, CONV_COLS), (8, LANES), (1, LANES), (1, LANES), (1, LANES)]
    return _Exchange([d_norms, d_conv, sums, d_fox_norm, d_gdn_norm, loss_row],
                     [jax.ShapeDtypeStruct((NDEV,) + s, F32) for s in shapes], n_remote + n_arrays, start, finish)


def _small_adamw(gathered, ws, ms, vs):
    n = len(SMALL_NAMES)
    ng = len(gathered)

    def body(*refs):
        def total(buf):
            acc = buf[0]
            for i in range(1, NDEV):
                acc = acc + buf[i]
            return acc

        t_norms, t_conv, t_sums, t_fn, t_gn, t_loss = [total(r) for r in refs[:ng]]
        w_refs, m_refs, v_refs = refs[ng:ng + n], refs[ng + n:ng + 2 * n], refs[ng + 2 * n:ng + 3 * n]
        outs = refs[ng + 3 * n:]
        outs[4 * n][...] = t_loss
        grads = [t_norms[i:i + 1, :] for i in range(4)] + [
            t_conv, t_sums[0:1, 0:NFH], t_sums[1:2, 0:NGH], t_sums[2:3, 0:NGH], t_fn[:, 0:FHD], t_gn]
        for a in range(n):
            d, nm, nv = _adamw_math(w_refs[a][...], grads[a], m_refs[a][...], v_refs[a][...])
            outs[a][...] = grads[a]
            outs[n + a][...] = d
            outs[2 * n + a][...] = nm
            outs[3 * n + a][...] = nv

    def whole(arr):
        return pl.BlockSpec(arr.shape, lambda i: (0,) * arr.ndim)

    res = pl.pallas_call(
        body, name="small_adamw", grid=(1,), in_specs=[whole(t) for t in gathered] + [whole(w) for w in ws] * 3,
        out_specs=[whole(w) for w in ws] * 4 + [pl.BlockSpec((1, LANES), lambda i: (0, 0))],
        out_shape=[jax.ShapeDtypeStruct(w.shape, F32) for w in ws] * 4 + [jax.ShapeDtypeStruct((1, LANES), F32)],
        compiler_params=_cparams(),
    )(*gathered, *ws, *ms, *vs)
    return res[:n], res[n:2 * n], res[2 * n:3 * n], res[3 * n:4 * n], res[4 * n]


CW = DPROJ // NCHIP
PROJ_RUNS = tuple((part * DFOX + hp * LANES, part * DFOX + (hp + 1) * LANES, (3 * hp + part) * LANES)
                  for hp in range(NPAIR) for part in range(3)) + (
    (1536, 1544, BLK_SMALL * LANES), (1544, 3080, BLK_GDN * LANES), (3080, 3088, BLK_SMALL * LANES + 8),
    (3088, 3600, BLK_GZ * LANES))


def _proj_pieces():
    pieces = []
    for lo, hi, at in PROJ_RUNS:
        while lo < hi:
            j = lo // CW
            end = min(hi, (j + 1) * CW)
            pieces.append((j, lo - j * CW, at, end - lo))
            at, lo = at + end - lo, end
    return pieces


RT = 256


def _to_padded_rows(gathered):
    def body(src_ref, out_ref, blocks_ref, rows_ref):
        blocks_ref[...] = src_ref[...].astype(F32)
        rows_ref[...] = jnp.zeros_like(rows_ref)
        for j, start, at, n in _proj_pieces():
            rows_ref[at:at + n, :] = blocks_ref[j, start:start + n, :]
        out_ref[...] = rows_ref[...].astype(out_ref.dtype)

    return pl.pallas_call(
        body, name="proj_rows_in", grid=(D // RT,), in_specs=[pl.BlockSpec((NCHIP, D, RT), lambda i: (0, 0, i))],
        out_specs=pl.BlockSpec((DPROJ_PAD, RT), lambda i: (0, i)), out_shape=jax.ShapeDtypeStruct((DPROJ_PAD, D), gathered.dtype),
        scratch_shapes=[pltpu.VMEM((NCHIP, D, RT), F32), pltpu.VMEM((DPROJ_PAD, RT), F32)], compiler_params=_cparams(),
    )(gathered)


def _from_padded_rows_pair_sum(w):
    steps = D // RT
    half = D // 2

    def body(src_ref, out_ref, rows_ref, blocks_ref, mine_ref, send_ref, recv_ref, send_sems, recv_sems):
        i = pl.program_id(0)
        x, y, c = _place()

        def share(t):
            return pltpu.make_async_remote_copy(
                src_ref=send_ref.at[t], dst_ref=recv_ref.at[t], send_sem=send_sems.at[t], recv_sem=recv_sems.at[t],
                device_id=(x, y, 1 - c), device_id_type=MESH)

        def rows_of(core):
            return blocks_ref[:, pl.ds(pl.multiple_of(core * half, half), half), :]

        @pl.when(i < steps)
        def _():
            rows_ref[...] = src_ref[...].astype(F32)
            blocks_ref[...] = jnp.zeros_like(blocks_ref)
            for j, start, at, n in _proj_pieces():
                blocks_ref[j, start:start + n, :] = rows_ref[at:at + n, :]
            mine_ref[i % 2] = rows_of(c)
            send_ref[i] = rows_of(1 - c).astype(BF16)
            share(i).start()

        @pl.when(i > 0)
        def _():
            share(i - 1).wait_recv()
            out_ref[...] = (mine_ref[(i - 1) % 2] + recv_ref[i - 1].astype(F32)).astype(BF16)

        @pl.when(i == steps)
        def _():
            for t in range(steps):
                share(t).wait_send()

    tile = (NCHIP, half, RT)
    return pl.pallas_call(
        body, name="proj_rows_out_pair_sum", grid=(steps + 1,),
        in_specs=[pl.BlockSpec((DPROJ_PAD, RT), lambda i: (0, jnp.minimum(i, steps - 1)))],
        out_specs=pl.BlockSpec(tile, lambda i: (0, 0, jnp.maximum(i - 1, 0))),
        out_shape=jax.ShapeDtypeStruct((NCHIP, half, D), BF16),
        scratch_shapes=[pltpu.VMEM((DPROJ_PAD, RT), F32), pltpu.VMEM((NCHIP, D, RT), F32), pltpu.VMEM((2,) + tile, F32),
                        pltpu.VMEM((steps,) + tile, BF16), pltpu.VMEM((steps,) + tile, BF16),
                        pltpu.SemaphoreType.DMA((steps,)), pltpu.SemaphoreType.DMA((steps,))],
        compiler_params=_cparams(),
    )(w)


def _local_step(x, target, first_weights, late_weights, reduce_late, reduce_in, pre_mix_norm, fox_f_bias, fox_out_norm,
                gdn_a_log, gdn_dt_bias, gdn_out_norm, post_mix_norm, pre_mlp_norm, post_mlp_norm):
    bias_vec = jnp.zeros((1, LANES), F32).at[0, 0:NFH].set(fox_f_bias).at[0, LANE_G:LANE_G + NGH].set(gdn_dt_bias)
    alog_vec = jnp.zeros((1, LANES), F32).at[0, LANE_G:LANE_G + NGH].set(gdn_a_log)
    w2 = jnp.concatenate([fox_out_norm, fox_out_norm], axis=1)

    h, first = _pre_norm(x, pre_mix_norm, exchange=first_weights[0])
    win_p, conv_w = first_weights[1](first)
    proj = _matmul(h, win_p, tb=True, tm=2048, tn=768, tk=1024, name="mm_proj")
    gates = _gates(proj, bias_vec, alog_vec)
    mix, fox_o, lse, late_b = _fox_fwd(proj, gates, w2, exchange=late_weights[1])
    qkv = _gdn_pre(proj, conv_w)
    (u, w, qd, kd, a_intra, gl, t_inv), late_a = _gdn_prep(qkv, gates, exchange=late_weights[0])
    wout, wup3 = late_weights[3](late_a, late_b)
    mix, gdn_raw, states = _gdn_scan(u, w, qd, kd, a_intra, gl, proj, gdn_out_norm, mix)
    def post_mix(acc, xv, w_post, w_pre_mlp):
        x1v = xv + acc * _rms_scale(acc) * w_post
        return acc, x1v, x1v * _rms_scale(x1v) * w_pre_mlp

    mixed, x1, h2 = _matmul(mix, wout, tm=512, tn=D, tk=1024, out_dtypes=(F32, F32, BF16), name="mm_out",
                            extra=(x, post_mix_norm, pre_mlp_norm), epilogue=post_mix)

    def relu2(acc):
        r = jnp.maximum(acc, 0.0)
        return r, r * r

    up_act = _matmul(h2, wup3, b3=True, tm=1024, tn=1024, tk=1024, out_dtypes=(BF16, BF16), epilogue=relu2,
                     name="mm_up", exchange=late_weights[2])
    (up_relu, act), late_c = up_act if late_weights[2] is not None else (up_act, [])
    wdown = late_weights[4](late_c)
    def loss_head(acc, x1v, tv, w):
        err = x1v + acc * _rms_scale(acc) * w - tv
        dx2v = err * (1.0 / D)
        dyv, dwt = _rms_bwd(acc, w, dx2v)
        part = 0.5 * jnp.sum(jnp.mean(err * err, axis=-1, keepdims=True), axis=0, keepdims=True)
        return dx2v, dyv, jnp.sum(dwt, axis=0, keepdims=True), jnp.broadcast_to(part, (1, D))

    dx2, dy, d_post_mlp, loss_wide = _matmul(
        act, wdown, tm=512, tn=D, tk=DFF, out_dtypes=(F32, BF16, F32, F32), extra=(x1, target, post_mlp_norm),
        epilogue=loss_head, n_sums=2, name="mm_down")
    loss_row = loss_wide[:, :LANES]

    dwdown = _matmul(act, dy, ta=True, tm=1024, tn=1024, tk=2048, out_dtypes=(BF16,), name="mm_dwdown")

    def relu2_bwd(acc, r):
        return (acc * 2.0 * r.astype(F32),)

    dup = _matmul(dy, wdown, tb=True, tm=1024, tn=1024, tk=1024, out_dtypes=(BF16,), extra=(up_relu,), epilogue=relu2_bwd,
                  name="mm_dact")
    dwup3 = _matmul(h2, dup, ta=True, tm=1024, tn=1024, tk=2048, out_dtypes=(BF16,), o3=True, name="mm_dwup")
    def mid_bwd(acc, x1v, dx2v, mixedv, w_pre_mlp, w_post):
        dxa, dwm = _rms_bwd(x1v, w_pre_mlp, acc)
        dx1v = dx2v + dxa
        dm, dwp = _rms_bwd(mixedv, w_post, dx1v)
        return dx1v, dm, jnp.sum(dwm, axis=0, keepdims=True), jnp.sum(dwp, axis=0, keepdims=True)

    dx1, dmixed, d_pre_mlp, d_post_mix = _matmul(
        dup, wup3, tb=True, b3=True, tm=512, tn=D, tk=DFF, out_dtypes=(F32, BF16, F32, F32),
        extra=(x1, dx2, mixed, pre_mlp_norm, post_mix_norm), epilogue=mid_bwd, n_sums=2, name="mm_dh2")
    dwout = _matmul(mix, dmixed, ta=True, tm=256, tn=1024, tk=2048, out_dtypes=(BF16,), name="mm_dwout")
    dmix = _matmul(dmixed, wout, tb=True, tm=512, tn=1024, tk=1024, name="mm_dmix")

    dfox, delta, d_fox_norm, from_sibling = _fox_norm_bwd(fox_o, dmix, w2, exchange=reduce_late[0](dwout, dwup3, dwdown))
    dproj, dcum_fox, reduced_a = _fox_bwd(proj, dfox, gates, lse, delta, exchange=reduce_late[1](from_sibling))
    (dproj, du, dw, dqd, dkd, da, dgl, d_gdn_norm), reduced_b = _gdn_scan_bwd(
        dmix, gdn_raw, proj, gdn_out_norm, u, w, qd, kd, a_intra, gl, states, dproj, exchange=reduce_late[2]())
    dqkv, dgates_gdn, reduced_c = _gdn_prep_bwd(qkv, gates, t_inv, du, dw, dqd, dkd, da, dgl, exchange=reduce_late[3]())
    reduced_late = (reduced_a, reduced_b, reduced_c)
    dproj, d_conv = _gdn_pre_bwd(proj, conv_w, dqkv, dproj)
    dproj, sums = _gates_bwd(proj, bias_vec, alog_vec, dgates_gdn, dcum_fox, dproj)

    dwin_p = _matmul(dproj, h, ta=True, tm=1280, tn=1024, tk=2048, out_dtypes=(BF16,), name="mm_dwin")
    exchange_in = reduce_in(dwin_p)
    dh = _matmul(dproj, win_p, tm=1024, tk=DPROJ_PAD, name="mm_dh", exchange=exchange_in)
    dh, reduced_in = dh if exchange_in is not None else (dh, [])
    grad_x, d_pre_mix = _pre_norm_bwd(dh, x, pre_mix_norm, dx1)

    d_norms = jnp.concatenate([d_pre_mix, d_post_mix, d_pre_mlp, d_post_mlp], axis=0)
    return grad_x, (d_norms, d_conv, sums, d_fox_norm, d_gdn_norm, loss_row), reduced_late, reduced_in


def kernel(x, pre_mix_norm, w_in, fox_f_bias, fox_out_norm, gdn_conv_w, gdn_a_log, gdn_dt_bias, gdn_out_norm, w_out, post_mix_norm, pre_mlp_norm, w_up, w_down, post_mlp_norm, loss_target, m_pre_mix_norm, m_w_in, m_fox_f_bias, m_fox_out_norm, m_gdn_conv_w, m_gdn_a_log, m_gdn_dt_bias, m_gdn_out_norm, m_w_out, m_post_mix_norm, m_pre_mlp_norm, m_w_up, m_w_down, m_post_mlp_norm, v_pre_mix_norm, v_w_in, v_fox_f_bias, v_fox_out_norm, v_gdn_conv_w, v_gdn_a_log, v_gdn_dt_bias, v_gdn_out_norm, v_w_out, v_post_mix_norm, v_pre_mlp_norm, v_w_up, v_w_down, v_post_mlp_norm):
    weights = dict(pre_mix_norm=pre_mix_norm, w_in=w_in, fox_f_bias=fox_f_bias, fox_out_norm=fox_out_norm, gdn_conv_w=gdn_conv_w,
                   gdn_a_log=gdn_a_log, gdn_dt_bias=gdn_dt_bias, gdn_out_norm=gdn_out_norm, w_out=w_out, post_mix_norm=post_mix_norm,
                   pre_mlp_norm=pre_mlp_norm, w_up=w_up, w_down=w_down, post_mlp_norm=post_mlp_norm)
    m_in = dict(pre_mix_norm=m_pre_mix_norm, w_in=m_w_in, fox_f_bias=m_fox_f_bias, fox_out_norm=m_fox_out_norm, gdn_conv_w=m_gdn_conv_w,
                gdn_a_log=m_gdn_a_log, gdn_dt_bias=m_gdn_dt_bias, gdn_out_norm=m_gdn_out_norm, w_out=m_w_out, post_mix_norm=m_post_mix_norm,
                pre_mlp_norm=m_pre_mlp_norm, w_up=m_w_up, w_down=m_w_down, post_mlp_norm=m_post_mlp_norm)
    v_in = dict(pre_mix_norm=v_pre_mix_norm, w_in=v_w_in, fox_f_bias=v_fox_f_bias, fox_out_norm=v_fox_out_norm, gdn_conv_w=v_gdn_conv_w,
                gdn_a_log=v_gdn_a_log, gdn_dt_bias=v_gdn_dt_bias, gdn_out_norm=v_gdn_out_norm, w_out=v_w_out, post_mix_norm=v_post_mix_norm,
                pre_mlp_norm=v_pre_mlp_norm, w_up=v_w_up, w_down=v_w_down, post_mlp_norm=v_post_mlp_norm)
    order_w = ("pre_mix_norm", "w_in", "fox_f_bias", "fox_out_norm", "gdn_conv_w", "gdn_a_log", "gdn_dt_bias", "gdn_out_norm", "w_out",
               "post_mix_norm", "pre_mlp_norm", "w_up", "w_down", "post_mlp_norm")
    big = ("w_in", "w_out", "w_up", "w_down")

    def row(v):
        return v if v.ndim == 2 else v.reshape(1, -1)

    win_shard = jnp.pad(w_in.T.astype(BF16), ((0, D - CW), (0, 0)))

    def resolve_first(gathered):
        win_g, conv_g = gathered
        return (_to_padded_rows(_with_own(win_g, win_shard)),
                _with_own(conv_g, gdn_conv_w).transpose(1, 0, 2).reshape(CONV_K, 3 * DGDN))

    late_shards = [weights[n].astype(BF16) for n in big[1:]]

    gathered_down = []

    def resolve_out_up(gathered_out, gathered_mlp):
        gathered_down.append(gathered_mlp[1])
        return _with_own(gathered_out[0], late_shards[0]).reshape(D, D), _with_own(gathered_mlp[0], late_shards[1])

    def resolve_down(_):
        return _with_own(gathered_down[0], late_shards[2]).reshape(DFF, D)

    pair_sums, late_blocks = {}, []

    def pair_summed(names, blocks, theirs):
        for n, s in zip(names, _pair_sum(blocks, theirs, "grads_pair_sum_" + names[0])):
            pair_sums[n] = s

    def late_pair_exchange(dwout, dwup3, dwdown):
        late_blocks.extend([dwout.reshape(NCHIP, D // NCHIP, D), dwup3, dwdown.reshape(NCHIP, DFF // NCHIP, D)])
        return _pair_exchange(late_blocks)

    def late_chip_exchange(theirs):
        pair_summed(big[1:], late_blocks, theirs)
        return _chip_exchange([pair_sums["w_up"], pair_sums["w_down"]])

    def reduce_in(dwin_p):
        pair_sums["w_in"] = _from_padded_rows_pair_sum(dwin_p)
        return _chip_exchange([pair_sums["w_in"]])

    grad_x, small, received_late, received_in = _local_step(
        x[0], loss_target[0], (_allgather_exchange([win_shard], whole=[gdn_conv_w]), resolve_first),
        (_allgather_exchange(late_shards[:1]), _allgather_exchange(late_shards[1:]), None, resolve_out_up, resolve_down),
        (late_pair_exchange, late_chip_exchange, lambda: None, lambda: _chip_exchange([pair_sums["w_out"]])),
        reduce_in, row(pre_mix_norm), fox_f_bias, row(fox_out_norm), gdn_a_log, gdn_dt_bias,
        row(gdn_out_norm), row(post_mix_norm), row(pre_mlp_norm), row(post_mlp_norm))
    received_mlp, _, received_out = received_late

    g_mine, g_theirs, small_gathered = _chip_sum(
        [pair_sums[n] for n in big], list(received_in[:1]) + list(received_out[:1]) + list(received_mlp[:2]),
        exchange=_small_gather(*small))

    g_big, d_big, nm_big, nv_big = _adamw_big(
        [weights[n] for n in big[1:]], g_mine[1:], g_theirs[1:], [m_in[n] for n in big[1:]], [v_in[n] for n in big[1:]])
    in_t = _adamw_in(w_in.T, g_mine[0], g_theirs[0], m_w_in.T, v_w_in.T)
    g_small, d_small, nm_small, nv_small, loss_total = _small_adamw(
        small_gathered, [row(weights[n]) for n in SMALL_NAMES], [row(m_in[n]) for n in SMALL_NAMES],
        [row(v_in[n]) for n in SMALL_NAMES])

    grads, delta, new_m, new_v = {}, {}, {}, {}
    grads["w_in"], delta["w_in"], new_m["w_in"], new_v["w_in"] = [t.T for t in in_t]
    for i, n in enumerate(big[1:]):
        grads[n], delta[n], new_m[n], new_v[n] = g_big[i], d_big[i], nm_big[i], nv_big[i]
    for i, n in enumerate(SMALL_NAMES):
        shape = weights[n].shape
        grads[n], delta[n], new_m[n], new_v[n] = (g_small[i].reshape(shape), d_small[i].reshape(shape),
                                                  nm_small[i].reshape(shape), nv_small[i].reshape(shape))
    return (loss_total[0, 0], grad_x[None], *[grads[n] for n in order_w], *[delta[n] for n in order_w], *[new_m[n] for n in order_w],
            *[new_v[n] for n in order_w])
```

```python
import jax
import jax.numpy as jnp
from jax import lax
from jax.experimental import pallas as pl
from jax.experimental.pallas import tpu as pltpu

F32 = jnp.float32
BF16 = jnp.bfloat16
MESH = pl.DeviceIdType.MESH

S = 2048
D = 1024
NFH, FHD = 8, 64
NPAIR = NFH // 2
NGH, GHD = 4, 128
DFOX = NFH * FHD
DGDN = NGH * GHD
CHUNK = 64
NCH = S // CHUNK
CONV_K = 4
DFF = 4 * D
EPS = 1e-6
DPROJ = 3600
LANES = 128
DPROJ_PAD = 3840
BLK_GDN = 12
BLK_GZ = 24
BLK_SMALL = 28
NCHIP = 4
NDEV = 8
VMEM_LIMIT = 56 * 1024 * 1024

ADAM_LR = 0.001
ADAM_B1 = 0.9
ADAM_B2 = 0.999
ADAM_EPS = 1e-08
ADAM_WD = 0.01
ADAM_STEP = 10


def _cparams(**kw):
    return pltpu.CompilerParams(vmem_limit_bytes=VMEM_LIMIT, **kw)


def _dn(ca, cb):
    return (((ca,), (cb,)), ((), ()))


def _dot(a, b, ca=1, cb=0):
    return lax.dot_general(a.astype(BF16), b.astype(BF16), _dn(ca, cb), preferred_element_type=F32)


def _hdot(a, b, ca=1, cb=0):
    return lax.dot_general(a.astype(F32), b.astype(F32), _dn(ca, cb), precision=lax.Precision.HIGHEST,
                           preferred_element_type=F32)


def _dot3(a, b, ca=1, cb=0):
    a_hi, b_hi = a.astype(BF16), b.astype(BF16)
    a_lo, b_lo = (a - a_hi.astype(F32)).astype(BF16), (b - b_hi.astype(F32)).astype(BF16)
    dn = _dn(ca, cb)
    return (lax.dot_general(a_hi, b_hi, dn, preferred_element_type=F32)
            + (lax.dot_general(a_hi, b_lo, dn, preferred_element_type=F32)
               + lax.dot_general(a_lo, b_hi, dn, preferred_element_type=F32)))


@jax.custom_vjp
def _mm_nn(a, b):
    return _dot(a, b, 1, 0)


def _mm_nn_fwd(a, b):
    return _dot(a, b, 1, 0), (a, b)


def _mm_nn_bwd(res, g):
    a, b = res
    return _dot(g, b, 1, 1), _dot(a, g, 0, 0)


_mm_nn.defvjp(_mm_nn_fwd, _mm_nn_bwd)


@jax.custom_vjp
def _mm_nt(a, b):
    return _dot(a, b, 1, 1)


def _mm_nt_fwd(a, b):
    return _dot(a, b, 1, 1), (a, b)


def _mm_nt_bwd(res, g):
    a, b = res
    return _dot(g, b, 1, 0), _dot(g, a, 0, 0)


_mm_nt.defvjp(_mm_nt_fwd, _mm_nt_bwd)


@jax.custom_vjp
def _saved_inverse(m, t_inv):
    del m
    return t_inv


def _saved_inverse_fwd(m, t_inv):
    del m
    return t_inv, t_inv


def _saved_inverse_bwd(t_inv, g):
    return -_dot3(_dot3(t_inv, g, 0, 0), t_inv, 1, 1), jnp.zeros_like(t_inv)


_saved_inverse.defvjp(_saved_inverse_fwd, _saved_inverse_bwd)


def _sigmoid(z):
    return 1.0 / (1.0 + jnp.exp(-z))


def _softplus(z):
    return jnp.maximum(z, 0.0) + jnp.log(1.0 + jnp.exp(-jnp.abs(z)))


def _silu(z):
    return z * _sigmoid(z)


def _rms_scale(x):
    return lax.rsqrt(jnp.mean(x * x, axis=-1, keepdims=True) + EPS)


def _rms_bwd(x, w, g):
    r = _rms_scale(x)
    gw = g * w
    dx = r * gw - x * (r * r * r) * jnp.mean(gw * x, axis=-1, keepdims=True)
    return dx, g * x * r


def _matmul(a, b, *, name, ta=False, tb=False, tm=512, tn=512, tk=512, out_dtypes=(F32,), b3=False, o3=False,
            extra=(), epilogue=None, exchange=None, n_sums=0):
    m, k = (a.shape[1], a.shape[0]) if ta else a.shape
    if b3:
        n = b.shape[1] if tb else b.shape[0] * b.shape[2]
        kb = b.shape[0] * b.shape[2] if tb else b.shape[1]
    else:
        n, kb = (b.shape[0], b.shape[1]) if tb else (b.shape[1], b.shape[0])
    assert kb == k, (name, kb, k)
    tm, tn, tk = min(tm, m), min(tn, n), min(tk, k)
    assert m % tm == 0 and n % tn == 0 and k % tk == 0, (name, m, n, k, tm, tn, tk)
    nk = k // tk
    whole_k_blocks = b3 and tb and not ta and nk == 1 and b.shape[0] > 1
    n_extra = len(extra)
    n_out = len(out_dtypes)
    grid = (m // tm, n // tn, nk)
    ex_in, ex_in_specs, ex_out_specs, ex_out_shape, ex_scratch = _hosted(exchange)

    def body(*refs):
        a_ref, b_ref = refs[0], refs[1]
        extra_refs = refs[2:2 + n_extra]
        first_out = 2 + n_extra + len(ex_in)
        out_refs = refs[first_out:first_out + n_out]
        ex_refs = refs[2 + n_extra:first_out] + refs[first_out + n_out:first_out + n_out + len(ex_out_shape)] + refs[-2:]
        step = [pl.program_id(d) for d in range(3)]

        if exchange is not None:
            @pl.when((step[0] == 0) & (step[1] == 0) & (step[2] == 0))
            def _():
                exchange.start(*exchange.split(ex_refs))

        def finish(acc):
            outs = (acc,) if epilogue is None else epilogue(acc, *[r[...] for r in extra_refs])
            for o_ref, val in zip(out_refs[:n_out - n_sums], outs):
                o_ref[...] = val.astype(o_ref.dtype)
            for o_ref, val in zip(out_refs[n_out - n_sums:], outs[n_out - n_sums:]):
                @pl.when(step[0] == 0)
                def _(o_ref=o_ref, val=val):
                    o_ref[...] = val

                @pl.when(step[0] > 0)
                def _(o_ref=o_ref, val=val):
                    o_ref[...] += val

        if whole_k_blocks:
            width = b.shape[2]
            part = _dot(a_ref[:, 0:width], b_ref[0], 1, 1)
            for blk in range(1, b.shape[0]):
                part = part + _dot(a_ref[:, blk * width:(blk + 1) * width], b_ref[blk], 1, 1)
        else:
            part = _dot(a_ref[...], b_ref[...], 0 if ta else 1, 1 if tb else 0)
        if nk == 1:
            finish(part)
        else:
            acc_ref = refs[first_out + n_out + len(ex_out_shape)]

            @pl.when(step[2] == 0)
            def _():
                acc_ref[...] = part

            @pl.when(step[2] > 0)
            def _():
                acc_ref[...] += part

            @pl.when(step[2] == nk - 1)
            def _():
                finish(acc_ref[...])

        if exchange is not None:
            flat = (step[0] * grid[1] + step[1]) * nk + step[2]
            total = grid[0] * grid[1] * nk

            @pl.when(flat == total // 2)
            def _():
                exchange.middle(*exchange.split(ex_refs))

            @pl.when(flat == total - 1)
            def _():
                exchange.rest(*exchange.split(ex_refs))

    a_spec = pl.BlockSpec((tk, tm), lambda i, j, kk: (kk, i)) if ta else pl.BlockSpec((tm, tk), lambda i, j, kk: (i, kk))
    if whole_k_blocks:
        b_spec = pl.BlockSpec((b.shape[0], tn, b.shape[2]), lambda i, j, kk: (0, j, 0))
    elif b3 and tb:
        assert b.shape[2] == tk
        b_spec = pl.BlockSpec((None, tn, tk), lambda i, j, kk: (kk, j, 0))
    elif b3:
        assert b.shape[2] == tn
        b_spec = pl.BlockSpec((None, tk, tn), lambda i, j, kk: (j, kk, 0))
    elif tb:
        b_spec = pl.BlockSpec((tn, tk), lambda i, j, kk: (j, kk))
    else:
        b_spec = pl.BlockSpec((tk, tn), lambda i, j, kk: (kk, j))
    tile = pl.BlockSpec((tm, tn), lambda i, j, kk: (i, j))
    out_specs = [tile] * n_out
    out_shape = [jax.ShapeDtypeStruct((m, n), dt) for dt in out_dtypes]
    if o3:
        out_specs[0] = pl.BlockSpec((None, tm, tn), lambda i, j, kk: (j, i, 0))
        out_shape[0] = jax.ShapeDtypeStruct((n // tn, m, tn), out_dtypes[0])
    assert n_sums == 0 or tn == n
    for r in range(n_out - n_sums, n_out):
        out_specs[r] = pl.BlockSpec((1, tn), lambda i, j, kk: (0, 0))
        out_shape[r] = jax.ShapeDtypeStruct((1, n), out_dtypes[r])
    res = pl.pallas_call(
        body, name=name, grid=grid,
        in_specs=[a_spec, b_spec] + [tile if e.shape[0] == m else pl.BlockSpec((1, tn), lambda i, j, kk: (0, j)) for e in extra]
        + ex_in_specs, out_specs=out_specs + ex_out_specs,
        out_shape=out_shape + ex_out_shape,
        scratch_shapes=([pltpu.VMEM((tm, tn), F32)] if nk > 1 else []) + ex_scratch,
        compiler_params=_cparams(),
    )(a, b, *extra, *ex_in)
    if exchange is not None:
        return (res[0] if n_out == 1 else res[:n_out]), res[n_out:]
    return res[0] if n_out == 1 else res


TR = 512


def _row_spec(cols):
    return pl.BlockSpec((TR, cols), lambda i: (i, 0))


def _vec_spec(cols):
    return pl.BlockSpec((1, cols), lambda i: (0, 0))


def _pre_norm(x, w, exchange=None):
    ex_in, ex_in_specs, ex_out_specs, ex_out_shape, ex_scratch = _hosted(exchange)

    def body(*refs):
        x_ref, w_ref, h_ref = refs[0], refs[1], refs[2 + len(ex_in)]
        ex_refs = refs[2:2 + len(ex_in)] + refs[3 + len(ex_in):]
        if exchange is not None:
            @pl.when(pl.program_id(0) == 0)
            def _():
                exchange.start(*exchange.split(ex_refs))

        xv = x_ref[...]
        h_ref[...] = (xv * _rms_scale(xv) * w_ref[...]).astype(BF16)

        if exchange is not None:
            @pl.when(pl.program_id(0) == S // TR - 1)
            def _():
                exchange.finish(*exchange.split(ex_refs))

    res = pl.pallas_call(
        body, name="pre_norm", grid=(S // TR,), in_specs=[_row_spec(D), _vec_spec(D)] + ex_in_specs,
        out_specs=[_row_spec(D)] + ex_out_specs, out_shape=[jax.ShapeDtypeStruct((S, D), BF16)] + ex_out_shape,
        scratch_shapes=ex_scratch, compiler_params=_cparams(),
    )(x, w, *ex_in)
    return res[0], res[1:]


def _pre_norm_bwd(dh, x, w, dx1):
    def body(dh_ref, x_ref, w_ref, dx1_ref, dx_ref, dw_ref):
        i = pl.program_id(0)
        dxa, dwt = _rms_bwd(x_ref[...], w_ref[...], dh_ref[...])
        dx_ref[...] = dx1_ref[...] + dxa

        @pl.when(i == 0)
        def _():
            dw_ref[...] = jnp.zeros_like(dw_ref)

        dw_ref[...] += jnp.sum(dwt, axis=0, keepdims=True)

    return pl.pallas_call(
        body, name="pre_norm_bwd", grid=(S // TR,),
        in_specs=[_row_spec(D), _row_spec(D), _vec_spec(D), _row_spec(D)], out_specs=[_row_spec(D), _vec_spec(D)],
        out_shape=[jax.ShapeDtypeStruct((S, D), F32), jax.ShapeDtypeStruct((1, D), F32)], compiler_params=_cparams(),
    )(dh, x, w, dx1)


BQ = 512
NQ = S // BQ
LANE_BETA, LANE_G = 8, 12


def _gate_lanes(shape):
    lane = lax.broadcasted_iota(jnp.int32, shape, 1)
    return lane < LANE_BETA, (lane >= LANE_BETA) & (lane < LANE_G), (lane >= LANE_G) & (lane < LANE_G + NGH)


def _gates(proj, bias_vec, alog_vec):
    def body(s_ref, b_ref, a_ref, o_ref, carry_ref):
        i = pl.program_id(0)

        @pl.when(i == 0)
        def _():
            carry_ref[...] = jnp.zeros_like(carry_ref)

        z = s_ref[...] + b_ref[...]
        tail = jnp.log(1.0 + jnp.exp(-jnp.abs(z)))
        sp = jnp.maximum(z, 0.0) + tail
        lf = jnp.minimum(z, 0.0) - tail
        r = lax.broadcasted_iota(jnp.int32, (BQ, BQ), 0)
        c = lax.broadcasted_iota(jnp.int32, (BQ, BQ), 1)
        tri = (c <= r).astype(F32)
        cum = _hdot(tri, lf) + carry_ref[...]
        carry_ref[...] = cum[BQ - 1:BQ, :]
        is_fox, is_beta, is_g = _gate_lanes(z.shape)
        o_ref[...] = jnp.where(is_fox, cum, jnp.where(is_beta, _sigmoid(z), jnp.where(is_g, -jnp.exp(a_ref[...]) * sp, 0.0)))

    return pl.pallas_call(
        body, name="gates", grid=(NQ,),
        in_specs=[pl.BlockSpec((BQ, LANES), lambda i: (i, BLK_SMALL)), _vec_spec(LANES), _vec_spec(LANES)],
        out_specs=pl.BlockSpec((BQ, LANES), lambda i: (i, 0)), out_shape=jax.ShapeDtypeStruct((S, LANES), F32),
        scratch_shapes=[pltpu.VMEM((1, LANES), F32)], compiler_params=_cparams(),
    )(proj, bias_vec, alog_vec)


def _gates_bwd(proj, bias_vec, alog_vec, dgates_gdn, dcum_fox, dproj):
    def body(s_ref, b_ref, a_ref, dg_ref, dc_ref, dproj_in, dproj_ref, red_ref, carry_ref):
        del dproj_in
        i = pl.program_id(0)

        @pl.when(i == 0)
        def _():
            carry_ref[...] = jnp.zeros_like(carry_ref)
            red_ref[...] = jnp.zeros_like(red_ref)

        z = s_ref[...] + b_ref[...]
        dg = dg_ref[...] + dc_ref[...]
        r = lax.broadcasted_iota(jnp.int32, (BQ, BQ), 0)
        c = lax.broadcasted_iota(jnp.int32, (BQ, BQ), 1)
        upper = (c >= r).astype(F32)
        dlf = _hdot(upper, dg) + carry_ref[...]
        carry_ref[...] = dlf[0:1, :]
        sig = _sigmoid(z)
        g_scale = -jnp.exp(a_ref[...])
        is_fox, is_beta, is_g = _gate_lanes(z.shape)
        ds = jnp.where(is_fox, dlf * (1.0 - sig), jnp.where(is_beta, dg * sig * (1.0 - sig), jnp.where(is_g, dg * g_scale * sig, 0.0)))
        dproj_ref[:, 0:LANES] = ds.astype(BF16)
        dproj_ref[:, LANES:2 * LANES] = jnp.zeros((BQ, LANES), BF16)
        dalog = jnp.where(is_g, dg * g_scale * _softplus(z), 0.0)
        sums = jnp.sum(ds, axis=0, keepdims=True)
        red_ref[0:1, :] += jnp.where(is_fox[0:1], sums, 0.0)
        red_ref[1:2, :] += pltpu.roll(jnp.where(is_g[0:1], sums, 0.0), LANES - LANE_G, 1)
        red_ref[2:3, :] += pltpu.roll(jnp.sum(dalog, axis=0, keepdims=True), LANES - LANE_G, 1)

    blk = pl.BlockSpec((BQ, LANES), lambda i: (NQ - 1 - i, 0))
    return pl.pallas_call(
        body, name="gates_bwd", grid=(NQ,),
        in_specs=[pl.BlockSpec((BQ, LANES), lambda i: (NQ - 1 - i, BLK_SMALL)), _vec_spec(LANES), _vec_spec(LANES), blk, blk,
                  pl.BlockSpec(memory_space=pl.ANY)],
        out_specs=[pl.BlockSpec((BQ, 2 * LANES), lambda i: (NQ - 1 - i, BLK_SMALL // 2)), pl.BlockSpec((8, LANES), lambda i: (0, 0))],
        out_shape=[jax.ShapeDtypeStruct((S, DPROJ_PAD), BF16), jax.ShapeDtypeStruct((8, LANES), F32)],
        input_output_aliases={5: 0},
        scratch_shapes=[pltpu.VMEM((1, LANES), F32)], compiler_params=_cparams(),
    )(proj, bias_vec, alog_vec, dgates_gdn, dcum_fox, dproj)


FOX_SCALE = FHD ** -0.5
FOX_PAIRS = 2
FOX_PAIRS_BWD = 2


def _head_mask(e):
    lane = lax.broadcasted_iota(jnp.int32, (1, LANES), 1)
    return (lane >= e * FHD) & (lane < (e + 1) * FHD)


def _lane_col(vals, index):
    lane = lax.broadcasted_iota(jnp.int32, vals.shape, 1)
    return jnp.sum(jnp.where(lane == index, vals, 0.0), axis=1, keepdims=True)


def _sublane_row(vals, index):
    row = lax.broadcasted_iota(jnp.int32, vals.shape, 0)
    return jnp.sum(jnp.where(row == index, vals, 0.0), axis=0, keepdims=True)


def _pair_cols(c0, c1):
    lane = lax.broadcasted_iota(jnp.int32, (c0.shape[0], 2), 1)
    return jnp.where(lane == 0, c0, c1)


def _split3(x):
    hi = x.astype(BF16).astype(F32)
    rest = x - hi
    mid = rest.astype(BF16).astype(F32)
    return hi, mid, (rest - mid).astype(BF16).astype(F32)


def _fox_operand(vals, e, cum, is_query):
    lane = lax.broadcasted_iota(jnp.int32, (1, LANES), 1)
    base = (1 - e) * FHD
    parts = _split3(cum)
    own = jnp.where(_head_mask(e), vals * FOX_SCALE if is_query else vals, 0.0)
    cum_at, ones_at = (base, base + 3) if is_query else (base + 3, base)
    sign = 1.0 if is_query else -1.0
    out = own + jnp.where((lane >= ones_at) & (lane < ones_at + 3), 1.0, 0.0)
    for i, part in enumerate(parts):
        out = out + jnp.where(lane == cum_at + i, sign * part, 0.0)
    return out.astype(BF16)


def _causal_block():
    return lax.broadcasted_iota(jnp.int32, (BQ, BQ), 1) <= lax.broadcasted_iota(jnp.int32, (BQ, BQ), 0)


def _head_rms(o, masks):
    o2 = o * o
    r = [lax.rsqrt(jnp.sum(jnp.where(mk, o2, 0.0), axis=1, keepdims=True) * (1.0 / FHD) + EPS) for mk in masks]
    return jnp.where(masks[0], r[0], r[1])


def _hosted(exchange):
    if exchange is None:
        return [], [], [], [], []
    return (exchange.inputs, [HBM] * len(exchange.inputs), [HBM] * len(exchange.out_shape), exchange.out_shape,
            exchange.sem_shapes())


def _fox_fwd(proj, gates, w2, exchange=None):
    ex_in, ex_in_specs, ex_out_specs, ex_out_shape, ex_scratch = _hosted(exchange)

    n_in = 3 * FOX_PAIRS + 2
    heads = [(pp, e) for pp in range(FOX_PAIRS) for e in range(2)]

    def body(*refs):
        qkv_refs, g_ref, w_ref = refs[:3 * FOX_PAIRS], refs[3 * FOX_PAIRS], refs[3 * FOX_PAIRS + 1]
        mix_ref, o_ref, lse_ref = refs[n_in + len(ex_in):n_in + 3 + len(ex_in)]
        ka_ref, vb_ref = refs[n_in + 3 + len(ex_in) + len(ex_out_shape):n_in + 5 + len(ex_in) + len(ex_out_shape)]
        ex_refs = refs[n_in:n_in + len(ex_in)] + refs[n_in + 3 + len(ex_in):n_in + 3 + len(ex_in) + len(ex_out_shape)] + refs[-2:]
        grp, qi = pl.program_id(0), pl.program_id(1)

        def head_index(pp, e):
            return 2 * (FOX_PAIRS * grp + pp) + e

        if exchange is not None:
            @pl.when((grp == 0) & (qi == 0))
            def _():
                exchange.start(*exchange.split(ex_refs))

        @pl.when(qi == 0)
        def _():
            gt = g_ref[...]
            for pp in range(FOX_PAIRS):
                kv = qkv_refs[3 * pp + 1][...]
                for e in range(2):
                    ka_ref[2 * pp + e] = _fox_operand(kv, e, _lane_col(gt, head_index(pp, e)), False)
                vb_ref[pp] = qkv_refs[3 * pp + 2][...].astype(BF16)

        masks = [_head_mask(0), _head_mask(1)]
        gt = g_ref[pl.ds(pl.multiple_of(qi * BQ, BQ), BQ), :]
        qs = [_fox_operand(qkv_refs[3 * pp][...], e, _lane_col(gt, head_index(pp, e)), True) for pp, e in heads]
        n = range(len(heads))

        def block(kj, carry, diagonal):
            rows = pl.ds(pl.multiple_of(kj * BQ, BQ), BQ)
            s = [_dot(qs[i], ka_ref[i, rows, :], 1, 1) for i in n]
            if diagonal:
                s = [jnp.where(_causal_block(), s[i], -jnp.inf) for i in n]
            m_new = [jnp.maximum(carry[i][0], jnp.max(s[i], axis=-1, keepdims=True)) for i in n]
            p = [jnp.exp(s[i] - m_new[i]) for i in n]
            alpha = [jnp.exp(carry[i][0] - m_new[i]) for i in n]
            l_new = [alpha[i] * carry[i][1] + jnp.sum(p[i], axis=-1, keepdims=True) for i in n]
            pv = [_dot(p[i], vb_ref[heads[i][0], rows, :]) for i in n]
            return tuple((m_new[i], l_new[i], alpha[i] * carry[i][2] + pv[i]) for i in n)

        one = (jnp.full((BQ, 1), -jnp.inf, F32), jnp.zeros((BQ, 1), F32), jnp.zeros((BQ, LANES), F32))
        below = lax.fori_loop(0, qi, lambda kj, carry: block(kj, carry, False), (one,) * len(heads))
        done = block(qi, below, True)
        for pp in range(FOX_PAIRS):
            (m0, l0, a0), (m1, l1, a1) = done[2 * pp], done[2 * pp + 1]
            o = jnp.where(masks[0], a0 / l0, a1 / l1)
            cols = slice(pp * LANES, (pp + 1) * LANES)
            o_ref[:, cols] = o
            mix_ref[:, cols] = (o * _head_rms(o, masks) * w_ref[...]).astype(BF16)
            lse_ref[pp] = _pair_cols(m0 + jnp.log(l0), m1 + jnp.log(l1))

        if exchange is not None:
            @pl.when((grp == NPAIR // FOX_PAIRS // 2) & (qi == 0))
            def _():
                exchange.middle(*exchange.split(ex_refs))

            @pl.when((grp == NPAIR // FOX_PAIRS - 1) & (qi == NQ - 1))
            def _():
                exchange.rest(*exchange.split(ex_refs))

    qkv_specs = []
    for pp in range(FOX_PAIRS):
        qkv_specs.append(pl.BlockSpec((BQ, LANES), lambda g, i, pp=pp: (i, 3 * (FOX_PAIRS * g + pp))))
        qkv_specs.append(pl.BlockSpec((S, LANES), lambda g, i, pp=pp: (0, 3 * (FOX_PAIRS * g + pp) + 1)))
        qkv_specs.append(pl.BlockSpec((S, LANES), lambda g, i, pp=pp: (0, 3 * (FOX_PAIRS * g + pp) + 2)))
    blk = pl.BlockSpec((BQ, FOX_PAIRS * LANES), lambda g, i: (i, g))
    res = pl.pallas_call(
        body, name="fox_fwd", grid=(NPAIR // FOX_PAIRS, NQ),
        in_specs=qkv_specs + [pl.BlockSpec((S, LANES), lambda g, i: (0, 0)), pl.BlockSpec((1, LANES), lambda g, i: (0, 0))]
        + ex_in_specs,
        out_specs=[blk, blk, pl.BlockSpec((FOX_PAIRS, BQ, 2), lambda g, i: (g, i, 0))] + ex_out_specs,
        out_shape=[jax.ShapeDtypeStruct((S, D), BF16), jax.ShapeDtypeStruct((S, DFOX), F32),
                   jax.ShapeDtypeStruct((NPAIR, S, 2), F32)] + ex_out_shape,
        scratch_shapes=[pltpu.VMEM((2 * FOX_PAIRS, S, LANES), BF16), pltpu.VMEM((FOX_PAIRS, S, LANES), BF16)] + ex_scratch,
        compiler_params=_cparams(),
    )(*([proj] * (3 * FOX_PAIRS)), gates, w2, *ex_in)
    return res[0], res[1], res[2], res[3:]


def _fox_norm_bwd(o, dmix, w2, exchange=None):
    ex_in, ex_in_specs, ex_out_specs, ex_out_shape, ex_scratch = _hosted(exchange)

    def body(*refs):
        o_ref, g_ref, w_ref = refs[:3]
        do_ref, dl_ref, dw_ref = refs[3 + len(ex_in):6 + len(ex_in)]
        ex_refs = refs[3:3 + len(ex_in)] + refs[6 + len(ex_in):]
        hp, qi = pl.program_id(0), pl.program_id(1)

        if exchange is not None:
            @pl.when((hp == 0) & (qi == 0))
            def _():
                exchange.start(*exchange.split(ex_refs))

        masks = [_head_mask(0), _head_mask(1)]
        ov = o_ref[...]
        g = g_ref[...]
        r = _head_rms(ov, masks)
        gw = g * w_ref[...]
        gwo = gw * ov
        mean = [jnp.sum(jnp.where(mk, gwo, 0.0), axis=1, keepdims=True) * (1.0 / FHD) for mk in masks]
        do = r * gw - ov * (r * r * r) * jnp.where(masks[0], mean[0], mean[1])
        do_ref[...] = do.astype(BF16)
        doo = do * ov
        dl_ref[...] = _pair_cols(*[jnp.sum(jnp.where(mk, doo, 0.0), axis=1, keepdims=True) for mk in masks])

        @pl.when((hp == 0) & (qi == 0))
        def _():
            dw_ref[...] = jnp.zeros_like(dw_ref)

        dw_ref[...] += jnp.sum(g * ov * r, axis=0, keepdims=True)

        @pl.when((hp == NPAIR - 1) & (qi == NQ - 1))
        def _():
            dw = dw_ref[...]
            dw_ref[...] = dw + pltpu.roll(dw, FHD, 1)
            if exchange is not None:
                exchange.finish(*exchange.split(ex_refs))

    blk = pl.BlockSpec((BQ, LANES), lambda hp, i: (i, hp))
    vec = pl.BlockSpec((1, LANES), lambda hp, i: (0, 0))
    res = pl.pallas_call(
        body, name="fox_norm_bwd", grid=(NPAIR, NQ), in_specs=[blk, blk, vec] + ex_in_specs,
        out_specs=[blk, pl.BlockSpec((None, BQ, 2), lambda hp, i: (hp, i, 0)), vec] + ex_out_specs,
        out_shape=[jax.ShapeDtypeStruct((S, DFOX), BF16), jax.ShapeDtypeStruct((NPAIR, S, 2), F32),
                   jax.ShapeDtypeStruct((1, LANES), F32)] + ex_out_shape,
        scratch_shapes=ex_scratch, compiler_params=_cparams(),
    )(o, dmix, w2, *ex_in)
    return res[0], res[1], res[2], res[3:]


def _fox_bwd(proj, do, gates, lse, delta, exchange=None):
    ex_in, ex_in_specs, ex_out_specs, ex_out_shape, ex_scratch = _hosted(exchange)

    pg = FOX_PAIRS_BWD
    n_in = 3 * pg + 4
    heads = [(pp, e) for pp in range(pg) for e in range(2)]

    def body(*refs):
        qkv_refs = refs[:3 * pg]
        do_ref, g_ref, lse_ref, dl_ref = refs[3 * pg:n_in]
        dproj_ref, dc_ref = refs[n_in + len(ex_in):n_in + 2 + len(ex_in)]
        qa_ref, dq_ref = refs[n_in + 2 + len(ex_in) + len(ex_out_shape):n_in + 4 + len(ex_in) + len(ex_out_shape)]
        ex_refs = refs[n_in:n_in + len(ex_in)] + refs[n_in + 2 + len(ex_in):n_in + 2 + len(ex_in) + len(ex_out_shape)] + refs[-2:]
        grp, kj = pl.program_id(0), pl.program_id(1)

        def head_index(pp, e):
            return 2 * (pg * grp + pp) + e

        if exchange is not None:
            @pl.when((grp == 0) & (kj == 0))
            def _():
                exchange.start(*exchange.split(ex_refs))

        @pl.when(kj == 0)
        def _():
            gt = g_ref[...]
            for pp in range(pg):
                qv = qkv_refs[3 * pp][...]
                for e in range(2):
                    qa_ref[2 * pp + e] = _fox_operand(qv, e, _lane_col(gt, head_index(pp, e)), True)
            dq_ref[...] = jnp.zeros_like(dq_ref)

        @pl.when((grp == 0) & (kj == 0))
        def _():
            dc_ref[...] = jnp.zeros_like(dc_ref)

        masks = [_head_mask(0), _head_mask(1)]
        krows = pl.ds(pl.multiple_of(kj * BQ, BQ), BQ)
        gk = g_ref[krows, :]
        kas = [_fox_operand(qkv_refs[3 * pp + 1][...], e, _lane_col(gk, head_index(pp, e)), False) for pp, e in heads]
        vbs = [qkv_refs[3 * pp + 2][...].astype(BF16) for pp in range(pg)]
        lane = lax.broadcasted_iota(jnp.int32, (BQ, LANES), 1)
        n = range(len(heads))

        def block(qi, carry, diagonal):
            dks, dvs, css = carry
            rows = pl.ds(pl.multiple_of(qi * BQ, BQ), BQ)
            qa = [qa_ref[i, rows, :] for i in n]
            s = [_dot(qa[i], kas[i], 1, 1) for i in n]
            if diagonal:
                s = [jnp.where(_causal_block(), s[i], -jnp.inf) for i in n]
            dov = [do_ref[rows, pp * LANES:(pp + 1) * LANES] for pp in range(pg)]
            doe = [jnp.where(masks[e], dov[pp], jnp.zeros_like(dov[pp])) for pp, e in heads]
            lse2 = [lse_ref[pp, rows, :] for pp in range(pg)]
            dl2 = [dl_ref[pp, rows, :] for pp in range(pg)]
            p = [jnp.exp(s[i] - _lane_col(lse2[heads[i][0]], heads[i][1])) for i in n]
            dp = [_dot(doe[i], vbs[heads[i][0]], 1, 1) for i in n]
            ds = [p[i] * (dp[i] - _lane_col(dl2[heads[i][0]], heads[i][1])) for i in n]
            dv_part = [_dot(p[i], doe[i], 0, 0) for i in n]
            dk_part = [_dot(ds[i], jnp.where(masks[heads[i][1]], qa[i], jnp.zeros_like(qa[i])), 0, 0) for i in n]
            dq_part = [jnp.where(masks[heads[i][1]], _dot(ds[i], kas[i]), 0.0) for i in n]
            css = tuple(css[i] + jnp.sum(ds[i], axis=0, keepdims=True) for i in n)
            dc = jnp.zeros((BQ, LANES), F32)
            for i in n:
                dc = dc + jnp.where(lane == head_index(*heads[i]), jnp.sum(ds[i], axis=1, keepdims=True), 0.0)
            for pp in range(pg):
                dq_ref[pp, rows, :] += (dq_part[2 * pp] + dq_part[2 * pp + 1]) * FOX_SCALE
            dc_ref[rows, :] += dc
            dks = tuple(dks[pp] + dk_part[2 * pp] + dk_part[2 * pp + 1] for pp in range(pg))
            dvs = tuple(dvs[pp] + dv_part[2 * pp] + dv_part[2 * pp + 1] for pp in range(pg))
            return dks, dvs, css

        zero = jnp.zeros((BQ, LANES), F32)
        first = block(kj, ((zero,) * pg, (zero,) * pg, (jnp.zeros((1, BQ), F32),) * len(heads)), True)
        dks, dvs, css = lax.fori_loop(kj + 1, NQ, lambda qi, carry: block(qi, carry, False), first)
        r = lax.broadcasted_iota(jnp.int32, (BQ, BQ), 0)
        c = lax.broadcasted_iota(jnp.int32, (BQ, BQ), 1)
        dcol = jnp.zeros((BQ, LANES), F32)
        for i in n:
            col = jnp.sum(jnp.where(r == c, css[i], 0.0), axis=1, keepdims=True)
            dcol = dcol + jnp.where(lane == head_index(*heads[i]), col, 0.0)
        dc_ref[krows, :] -= dcol
        for pp in range(pg):
            base = 3 * pp * LANES
            dproj_ref[krows, base + LANES:base + 2 * LANES] = dks[pp].astype(BF16)
            dproj_ref[krows, base + 2 * LANES:base + 3 * LANES] = dvs[pp].astype(BF16)

        @pl.when(kj == NQ - 1)
        def _():
            for pp in range(pg):
                dproj_ref[:, 3 * pp * LANES:(3 * pp + 1) * LANES] = dq_ref[pp].astype(BF16)

        if exchange is not None:
            @pl.when((grp == NPAIR // pg // 2) & (kj == 0))
            def _():
                exchange.middle(*exchange.split(ex_refs))

            @pl.when((grp == NPAIR // pg - 1) & (kj == NQ - 1))
            def _():
                exchange.rest(*exchange.split(ex_refs))

    qkv_specs = []
    for pp in range(pg):
        qkv_specs.append(pl.BlockSpec((S, LANES), lambda g, j, pp=pp: (0, 3 * (pg * g + pp))))
        qkv_specs.append(pl.BlockSpec((BQ, LANES), lambda g, j, pp=pp: (j, 3 * (pg * g + pp) + 1)))
        qkv_specs.append(pl.BlockSpec((BQ, LANES), lambda g, j, pp=pp: (j, 3 * (pg * g + pp) + 2)))
    pair = pl.BlockSpec((pg, S, 2), lambda g, j: (g, 0, 0))
    res = pl.pallas_call(
        body, name="fox_bwd", grid=(NPAIR // pg, NQ),
        in_specs=qkv_specs + [pl.BlockSpec((S, pg * LANES), lambda g, j: (0, g)), pl.BlockSpec((S, LANES), lambda g, j: (0, 0)),
                              pair, pair] + ex_in_specs,
        out_specs=[pl.BlockSpec((S, 3 * pg * LANES), lambda g, j: (0, g)), pl.BlockSpec((S, LANES), lambda g, j: (0, 0))]
        + ex_out_specs,
        out_shape=[jax.ShapeDtypeStruct((S, DPROJ_PAD), BF16), jax.ShapeDtypeStruct((S, LANES), F32)] + ex_out_shape,
        scratch_shapes=[pltpu.VMEM((2 * pg, S, LANES), BF16), pltpu.VMEM((pg, S, LANES), F32)] + ex_scratch,
        compiler_params=_cparams(),
    )(*([proj] * (3 * pg)), do, gates, lse, delta, *ex_in)
    return res[0], res[1], res[2:]


NQKV = 3 * NGH
GDN_QSCALE = GHD ** -0.5


def _shift_down(x, s):
    if s == 0:
        return x
    row = lax.broadcasted_iota(jnp.int32, x.shape, 0)
    return jnp.where(row >= s, pltpu.roll(x, s, 0), 0.0)


def _shift_up(x, s):
    if s == 0:
        return x
    n = x.shape[0]
    row = lax.broadcasted_iota(jnp.int32, x.shape, 0)
    return jnp.where(row < n - s, pltpu.roll(x, n - s, 0), 0.0)


def _conv_taps(xv):
    return [_shift_down(xv, CONV_K - 1 - j) for j in range(CONV_K)]


def _conv_pre(taps, wv):
    pre = taps[CONV_K - 1] * wv[CONV_K - 1:CONV_K, :]
    for j in range(CONV_K - 1):
        pre = pre + taps[j] * wv[j:j + 1, :]
    return pre


def _l2_factors(b):
    return b < 2 * NGH, jnp.where(b < NGH, GDN_QSCALE, 1.0)


def _gdn_pre(proj, conv_w):
    def body(x_ref, w_ref, o_ref):
        b = pl.program_id(0)
        c = _silu(_conv_pre(_conv_taps(x_ref[...]), w_ref[...]))
        normed, scale = _l2_factors(b)
        rs = lax.rsqrt(jnp.sum(c * c, axis=-1, keepdims=True) + EPS)
        o_ref[...] = c * jnp.where(normed, rs, 1.0) * scale

    return pl.pallas_call(
        body, name="gdn_pre", grid=(NQKV,),
        in_specs=[pl.BlockSpec((S, GHD), lambda b: (0, BLK_GDN + b)), pl.BlockSpec((CONV_K, GHD), lambda b: (0, b))],
        out_specs=pl.BlockSpec((S, GHD), lambda b: (0, b)),
        out_shape=jax.ShapeDtypeStruct((S, NQKV * GHD), F32), compiler_params=_cparams(),
    )(proj, conv_w)


def _gdn_pre_bwd(proj, conv_w, dqkv, dproj):
    def body(x_ref, w_ref, dy_ref, dproj_in, dx_ref, dw_ref):
        del dproj_in
        b = pl.program_id(0)
        taps = _conv_taps(x_ref[...])
        wv = w_ref[...]
        pre = _conv_pre(taps, wv)
        sig = _sigmoid(pre)
        c = pre * sig
        normed, scale = _l2_factors(b)
        g = dy_ref[...] * scale
        rs = lax.rsqrt(jnp.sum(c * c, axis=-1, keepdims=True) + EPS)
        dc_n = rs * g - c * (rs * rs * rs) * jnp.sum(g * c, axis=-1, keepdims=True)
        dc = jnp.where(normed, dc_n, g)
        dpre = dc * sig * (1.0 + pre * (1.0 - sig))
        dx = dpre * wv[CONV_K - 1:CONV_K, :]
        for j in range(CONV_K - 1):
            dx = dx + _shift_up(dpre, CONV_K - 1 - j) * wv[j:j + 1, :]
        dx_ref[...] = dx.astype(BF16)
        for j in range(CONV_K):
            dw_ref[j:j + 1, :] = jnp.sum(dpre * taps[j], axis=0, keepdims=True)

    return pl.pallas_call(
        body, name="gdn_pre_bwd", grid=(NQKV,),
        in_specs=[pl.BlockSpec((S, GHD), lambda b: (0, BLK_GDN + b)), pl.BlockSpec((CONV_K, GHD), lambda b: (0, b)),
                  pl.BlockSpec((None, S, GHD), lambda b: (b // NGH, 0, b % NGH)), pl.BlockSpec(memory_space=pl.ANY)],
        out_specs=[pl.BlockSpec((S, GHD), lambda b: (0, BLK_GDN + b)), pl.BlockSpec((CONV_K, GHD), lambda b: (0, b))],
        out_shape=[jax.ShapeDtypeStruct((S, DPROJ_PAD), BF16), jax.ShapeDtypeStruct((CONV_K, NQKV * GHD), F32)],
        input_output_aliases={3: 0}, compiler_params=_cparams(),
    )(proj, conv_w, dqkv, dproj)


CB = 16
NCB = NCH // CB


def _chunk_prep(qs, ks, vs, gcols, bcols, t_saved=None):
    n = range(len(qs))
    r = lax.broadcasted_iota(jnp.int32, (CHUNK, CHUNK), 0)
    c = lax.broadcasted_iota(jnp.int32, (CHUNK, CHUNK), 1)
    incl = c <= r
    eye = (r == c).astype(F32)
    grow = [jnp.sum(gcols[i] * eye, axis=0, keepdims=True) for i in n]
    gc_col = [jnp.sum(jnp.where(incl, grow[i], 0.0), axis=1, keepdims=True) for i in n]
    gc_row = [jnp.sum(jnp.where(r <= c, gcols[i], 0.0), axis=0, keepdims=True) for i in n]
    decay = [jnp.exp(jnp.where(incl, gc_col[i] - gc_row[i], -jnp.inf)) for i in n]
    kb = [ks[i] * bcols[i] for i in n]
    vb = [vs[i] * bcols[i] for i in n]
    kk = [_mm_nt(kb[i], ks[i]) for i in n]
    m = [jnp.where(c < r, kk[i] * decay[i], 0.0) for i in n]
    if t_saved is None:
        t_inv = [eye - m[i] for i in n]
        p = [_dot3(m[i], m[i]) for i in n]
        for step in range(5):
            t_inv = [t_inv[i] + _dot3(t_inv[i], p[i]) for i in n]
            if step < 4:
                p = [_dot3(p[i], p[i]) for i in n]
    else:
        t_inv = [_saved_inverse(m[i], t_saved[i]) for i in n]
    egc = [jnp.exp(gc_col[i]) for i in n]
    u = [_mm_nn(t_inv[i], vb[i]) for i in n]
    w = [_mm_nn(t_inv[i], kb[i] * egc[i]) for i in n]
    qk = [_mm_nt(qs[i], ks[i]) for i in n]
    gc_last = [gc_col[i][CHUNK - 1:CHUNK, :] for i in n]
    return [(u[i], w[i], qk[i] * decay[i], qs[i] * egc[i], ks[i] * jnp.exp(gc_last[i] - gc_col[i]), jnp.exp(gc_last[i]),
             t_inv[i]) for i in n]


def _prep_specs():
    rows = CB * CHUNK
    qs = pl.BlockSpec((rows, GHD), lambda i, h: (i, h))
    ks = pl.BlockSpec((rows, GHD), lambda i, h: (i, NGH + h))
    vs = pl.BlockSpec((rows, GHD), lambda i, h: (i, 2 * NGH + h))
    gs = pl.BlockSpec((rows, LANES), lambda i, h: (i, 0))
    a_s = pl.BlockSpec((None, rows, CHUNK), lambda i, h: (h, i, 0))
    gl_s = pl.BlockSpec((None, CB, 1, LANES), lambda i, h: (h, i, 0, 0))
    return qs, ks, vs, gs, a_s, gl_s


def _gdn_prep(qkv, gates, exchange=None):
    ex_in, ex_in_specs, ex_out_specs, ex_out_shape, ex_scratch = _hosted(exchange)

    def body(*refs):
        q_ref, k_ref, v_ref, g_ref = refs[:4]
        u_ref, w_ref, qd_ref, kd_ref, a_ref, gl_ref, t_ref = refs[4 + len(ex_in):11 + len(ex_in)]
        ex_refs = refs[4:4 + len(ex_in)] + refs[11 + len(ex_in):]
        h = pl.program_id(1)

        if exchange is not None:
            @pl.when((pl.program_id(0) == 0) & (h == 0))
            def _():
                exchange.start(*exchange.split(ex_refs))

        chunks = [pl.ds(cidx * CHUNK, CHUNK) for cidx in range(CB)]
        gts = [g_ref[rows, :] for rows in chunks]
        outs = _chunk_prep([q_ref[rows, :] for rows in chunks], [k_ref[rows, :] for rows in chunks],
                           [v_ref[rows, :] for rows in chunks], [_lane_col(gt, LANE_G + h) for gt in gts],
                           [_lane_col(gt, LANE_BETA + h) for gt in gts])
        for cidx, rows in enumerate(chunks):
            u, w, a, qd, kd, gl, t_inv = outs[cidx]
            u_ref[rows, :] = u
            w_ref[rows, :] = w
            qd_ref[rows, :] = qd
            kd_ref[rows, :] = kd
            a_ref[rows, :] = a
            t_ref[rows, :] = t_inv
            gl_ref[cidx] = jnp.broadcast_to(gl, (1, LANES))

        if exchange is not None:
            @pl.when((pl.program_id(0) == NCB // 2) & (h == 0))
            def _():
                exchange.middle(*exchange.split(ex_refs))

            @pl.when((pl.program_id(0) == NCB - 1) & (h == NGH - 1))
            def _():
                exchange.rest(*exchange.split(ex_refs))

    qs, ks, vs, gs, a_s, gl_s = _prep_specs()
    tok = jax.ShapeDtypeStruct((S, DGDN), F32)
    sq = jax.ShapeDtypeStruct((NGH, S, CHUNK), F32)
    res = pl.pallas_call(
        body, name="gdn_prep", grid=(NCB, NGH), in_specs=[qs, ks, vs, gs] + ex_in_specs,
        out_specs=[qs, qs, qs, qs, a_s, gl_s, a_s] + ex_out_specs,
        out_shape=[tok, tok, tok, tok, sq, jax.ShapeDtypeStruct((NGH, NCH, 1, LANES), F32), sq] + ex_out_shape,
        scratch_shapes=ex_scratch, compiler_params=_cparams(),
    )(qkv, qkv, qkv, gates, *ex_in)
    return res[:7], res[7:]


def _gdn_prep_bwd(qkv, gates, t_inv, du, dw, dqd, dkd, da, dgl, exchange=None):
    ex_in, ex_in_specs, ex_out_specs, ex_out_shape, ex_scratch = _hosted(exchange)

    def body(*refs):
        q_ref, k_ref, v_ref, g_ref, t_ref, du_ref, dw_ref, dqd_ref, dkd_ref, da_ref, dgl_ref = refs[:11]
        dqkv_ref, dg_ref = refs[11 + len(ex_in):13 + len(ex_in)]
        ex_refs = refs[11:11 + len(ex_in)] + refs[13 + len(ex_in):]
        h = pl.program_id(1)

        if exchange is not None:
            @pl.when((pl.program_id(0) == 0) & (h == 0))
            def _():
                exchange.start(*exchange.split(ex_refs))

        @pl.when(h == 0)
        def _():
            dg_ref[...] = jnp.zeros_like(dg_ref)

        lane = lax.broadcasted_iota(jnp.int32, (CHUNK, LANES), 1)
        chunks = [pl.ds(cidx * CHUNK, CHUNK) for cidx in range(CB)]
        gts = [g_ref[rows, :] for rows in chunks]
        t_saved = [t_ref[rows, :] for rows in chunks]
        _, vjp = jax.vjp(lambda *args: [o[:6] for o in _chunk_prep(*args, t_saved=t_saved)],
                         [q_ref[rows, :] for rows in chunks], [k_ref[rows, :] for rows in chunks],
                         [v_ref[rows, :] for rows in chunks], [_lane_col(gt, LANE_G + h) for gt in gts],
                         [_lane_col(gt, LANE_BETA + h) for gt in gts])
        dqs, dks, dvs, dgcs, dbcs = vjp([(du_ref[rows, :], dw_ref[rows, :], da_ref[rows, :], dqd_ref[rows, :],
                                          dkd_ref[rows, :], dgl_ref[cidx][:, 0:1]) for cidx, rows in enumerate(chunks)])
        for cidx, rows in enumerate(chunks):
            dq, dk, dv, dgc, dbc = dqs[cidx], dks[cidx], dvs[cidx], dgcs[cidx], dbcs[cidx]
            dqkv_ref[0, rows, :] = dq
            dqkv_ref[1, rows, :] = dk
            dqkv_ref[2, rows, :] = dv
            dg_ref[rows, :] += jnp.where(lane == LANE_G + h, dgc, 0.0) + jnp.where(lane == LANE_BETA + h, dbc, 0.0)

        if exchange is not None:
            @pl.when((pl.program_id(0) == NCB - 1) & (h == NGH - 1))
            def _():
                exchange.finish(*exchange.split(ex_refs))

    qs, ks, vs, gs, a_s, gl_s = _prep_specs()
    res = pl.pallas_call(
        body, name="gdn_prep_bwd", grid=(NCB, NGH), in_specs=[qs, ks, vs, gs, a_s, qs, qs, qs, qs, a_s, gl_s] + ex_in_specs,
        out_specs=[pl.BlockSpec((3, CB * CHUNK, GHD), lambda i, h: (0, i, h)), gs] + ex_out_specs,
        out_shape=[jax.ShapeDtypeStruct((3, S, DGDN), F32), jax.ShapeDtypeStruct((S, LANES), F32)] + ex_out_shape,
        scratch_shapes=ex_scratch, compiler_params=_cparams(),
    )(qkv, qkv, qkv, gates, t_inv, du, dw, dqd, dkd, da, dgl, *ex_in)
    return res[0], res[1], res[2:]


def _scan_specs(nh, parts, reverse):
    wide, rows, chunks = nh * GHD, S // parts, NCH // parts

    def part(p):
        return parts - 1 - p if reverse else p

    hs = pl.BlockSpec((rows, wide), lambda g, p: (part(p), g))
    a_s = pl.BlockSpec((nh, rows, CHUNK), lambda g, p: (g, part(p), 0))
    gl_s = pl.BlockSpec((nh, chunks, 1, LANES), lambda g, p: (g, part(p), 0, 0))
    st_s = pl.BlockSpec((nh, chunks, GHD, GHD), lambda g, p: (g, part(p), 0, 0))
    gz_s = pl.BlockSpec((rows, wide), lambda g, p: (part(p), BLK_GZ // nh + g))
    mix_s = pl.BlockSpec((rows, wide), lambda g, p: (part(p), NPAIR // nh + g))
    return hs, a_s, gl_s, st_s, gz_s, mix_s


def _head_cols(hh):
    return slice(hh * GHD, (hh + 1) * GHD)


SCAN_HEADS, SCAN_PARTS = 4, 2
SCAN_HEADS_BWD, SCAN_PARTS_BWD = 4, 4


def _gdn_scan(u, w, qd, kd, a, gl, proj, w_norm, mix):
    heads = range(SCAN_HEADS)

    def body(u_ref, w_ref, qd_ref, kd_ref, a_ref, gl_ref, z_ref, wn_ref, mix_in, mix_ref, o_ref, st_ref, carry_ref):
        del mix_in

        @pl.when(pl.program_id(1) == 0)
        def _():
            carry_ref[...] = jnp.zeros_like(carry_ref)

        def step(ci, states):
            rows = pl.ds(pl.multiple_of(ci * CHUNK, CHUNK), CHUNK)
            for hh in heads:
                st_ref[hh, ci] = states[hh]
            ws = [_dot(w_ref[rows, _head_cols(hh)], states[hh]) for hh in heads]
            qs = [_dot(qd_ref[rows, _head_cols(hh)], states[hh]) for hh in heads]
            vn = [u_ref[rows, _head_cols(hh)] - ws[hh] for hh in heads]
            av = [_dot(a_ref[hh, rows, :], vn[hh]) for hh in heads]
            kv = [_dot(kd_ref[rows, _head_cols(hh)], vn[hh], 0, 0) for hh in heads]
            for hh in heads:
                o_ref[rows, _head_cols(hh)] = qs[hh] + av[hh]
            return tuple(states[hh] * gl_ref[hh, ci] + kv[hh] for hh in heads)

        last = lax.fori_loop(0, NCH // SCAN_PARTS, step, tuple(carry_ref[hh] for hh in heads))
        for hh in heads:
            carry_ref[hh] = last[hh]
            ov = o_ref[:, _head_cols(hh)]
            mix_ref[:, _head_cols(hh)] = (ov * _rms_scale(ov) * wn_ref[...] * _silu(z_ref[:, _head_cols(hh)])).astype(BF16)

    hs, a_s, gl_s, st_s, gz_s, mix_s = _scan_specs(SCAN_HEADS, SCAN_PARTS, False)
    return pl.pallas_call(
        body, name="gdn_scan", grid=(NGH // SCAN_HEADS, SCAN_PARTS),
        in_specs=[hs, hs, hs, hs, a_s, gl_s, gz_s, pl.BlockSpec((1, GHD), lambda g, p: (0, 0)),
                  pl.BlockSpec(memory_space=pl.ANY)],
        out_specs=[mix_s, hs, st_s],
        out_shape=[jax.ShapeDtypeStruct((S, D), BF16), jax.ShapeDtypeStruct((S, DGDN), F32),
                   jax.ShapeDtypeStruct((NGH, NCH, GHD, GHD), F32)],
        input_output_aliases={8: 0}, scratch_shapes=[pltpu.VMEM((SCAN_HEADS, GHD, GHD), F32)], compiler_params=_cparams(),
    )(u, w, qd, kd, a, gl, proj, w_norm, mix)


def _gdn_scan_bwd(dmix, o, proj, w_norm, u, w, qd, kd, a, gl, states, dproj, exchange=None):
    ex_in, ex_in_specs, ex_out_specs, ex_out_shape, ex_scratch = _hosted(exchange)
    groups = NGH // SCAN_HEADS_BWD

    def body(*refs):
        dy_ref, o_ref, z_ref, wn_ref, u_ref, w_ref, qd_ref, kd_ref, a_ref, gl_ref, st_ref = refs[:11]
        dz_ref, du_ref, dw_ref, dqd_ref, dkd_ref, da_ref, dgl_ref, dwn_ref = refs[12 + len(ex_in):20 + len(ex_in)]
        do_ref, carry_ref = refs[20 + len(ex_in) + len(ex_out_shape):22 + len(ex_in) + len(ex_out_shape)]
        ex_refs = refs[12:12 + len(ex_in)] + refs[20 + len(ex_in):20 + len(ex_in) + len(ex_out_shape)] + refs[-2:]
        heads = range(SCAN_HEADS_BWD)
        chunks = NCH // SCAN_PARTS_BWD

        if exchange is not None:
            @pl.when((pl.program_id(0) == 0) & (pl.program_id(1) == 0))
            def _():
                exchange.start(*exchange.split(ex_refs))

        @pl.when((pl.program_id(0) == 0) & (pl.program_id(1) == 0))
        def _():
            dwn_ref[...] = jnp.zeros_like(dwn_ref)

        @pl.when(pl.program_id(1) == 0)
        def _():
            carry_ref[...] = jnp.zeros_like(carry_ref)

        wn = wn_ref[...]
        for hh in heads:
            c = _head_cols(hh)
            ov = o_ref[:, c]
            zv = z_ref[:, c]
            g = dy_ref[:, c]
            sig = _sigmoid(zv)
            dz_ref[:, c] = (g * (ov * _rms_scale(ov) * wn) * sig * (1.0 + zv * (1.0 - sig))).astype(BF16)
            do, dwt = _rms_bwd(ov, wn, g * zv * sig)
            do_ref[:, c] = do
            dwn_ref[...] += jnp.sum(dwt, axis=0, keepdims=True)

        def step(t, dstates):
            ci = chunks - 1 - t
            rows = pl.ds(pl.multiple_of(ci * CHUNK, CHUNK), CHUNK)
            cols = [_head_cols(hh) for hh in heads]
            state = [st_ref[hh, ci] for hh in heads]
            dov = [do_ref[rows, cols[hh]] for hh in heads]
            wv = [w_ref[rows, cols[hh]] for hh in heads]
            ws = [_dot(wv[hh], state[hh]) for hh in heads]
            adov = [_dot(a_ref[hh, rows, :], dov[hh], 0, 0) for hh in heads]
            kds = [_dot(kd_ref[rows, cols[hh]], dstates[hh]) for hh in heads]
            dqd = [_dot(dov[hh], state[hh], 1, 1) for hh in heads]
            qdo = [_dot(qd_ref[rows, cols[hh]], dov[hh], 0, 0) for hh in heads]
            vn = [u_ref[rows, cols[hh]] - ws[hh] for hh in heads]
            dvn = [adov[hh] + kds[hh] for hh in heads]
            da = [_dot(dov[hh], vn[hh], 1, 1) for hh in heads]
            dkd = [_dot(vn[hh], dstates[hh], 1, 1) for hh in heads]
            dwv = [_dot(dvn[hh], state[hh], 1, 1) for hh in heads]
            wdv = [_dot(wv[hh], dvn[hh], 0, 0) for hh in heads]
            for hh in heads:
                da_ref[hh, rows, :] = da[hh]
                dqd_ref[rows, cols[hh]] = dqd[hh]
                dkd_ref[rows, cols[hh]] = dkd[hh]
                dgl = jnp.sum(jnp.sum(dstates[hh] * state[hh], axis=1, keepdims=True), axis=0, keepdims=True)
                dgl_ref[hh, ci] = jnp.broadcast_to(dgl, (1, LANES))
                du_ref[rows, cols[hh]] = dvn[hh]
                dw_ref[rows, cols[hh]] = -dwv[hh]
            return tuple(dstates[hh] * gl_ref[hh, ci] + qdo[hh] - wdv[hh] for hh in heads)

        last = lax.fori_loop(0, chunks, step, tuple(carry_ref[hh] for hh in heads))
        for hh in heads:
            carry_ref[hh] = last[hh]

        if exchange is not None:
            @pl.when((pl.program_id(0) == groups - 1) & (pl.program_id(1) == SCAN_PARTS_BWD - 1))
            def _():
                exchange.finish(*exchange.split(ex_refs))

    hs, a_s, gl_s, st_s, gz_s, mix_s = _scan_specs(SCAN_HEADS_BWD, SCAN_PARTS_BWD, True)
    vec = pl.BlockSpec((1, GHD), lambda g, p: (0, 0))
    tok = jax.ShapeDtypeStruct((S, DGDN), F32)
    res = pl.pallas_call(
        body, name="gdn_scan_bwd", grid=(groups, SCAN_PARTS_BWD),
        in_specs=[mix_s, hs, gz_s, vec, hs, hs, hs, hs, a_s, gl_s, st_s, pl.BlockSpec(memory_space=pl.ANY)] + ex_in_specs,
        out_specs=[gz_s, hs, hs, hs, hs, a_s, gl_s, vec] + ex_out_specs,
        out_shape=[jax.ShapeDtypeStruct((S, DPROJ_PAD), BF16), tok, tok, tok, tok,
                   jax.ShapeDtypeStruct((NGH, S, CHUNK), F32), jax.ShapeDtypeStruct((NGH, NCH, 1, LANES), F32),
                   jax.ShapeDtypeStruct((1, GHD), F32)] + ex_out_shape,
        input_output_aliases={11: 0},
        scratch_shapes=[pltpu.VMEM((S // SCAN_PARTS_BWD, SCAN_HEADS_BWD * GHD), F32),
                        pltpu.VMEM((SCAN_HEADS_BWD, GHD, GHD), F32)] + ex_scratch,
        compiler_params=_cparams(),
    )(dmix, o, proj, w_norm, u, w, qd, kd, a, gl, states, dproj, *ex_in)
    return res[:8], res[8:]


def _place():
    return lax.axis_index("x"), lax.axis_index("y"), lax.axis_index("c")


def _place_scalars():
    x, y, c = _place()
    return jnp.stack([2 * x + y, c]).astype(jnp.int32)


def _other_chips(x, y):
    return [(1 - x, y), (x, 1 - y), (1 - x, 1 - y)]


HBM = pl.BlockSpec(memory_space=pltpu.HBM)
VMEM = pl.BlockSpec(memory_space=pltpu.VMEM)


def _half_rows(ref_or_rows, half):
    rows = ref_or_rows // 2
    return pl.ds(pl.multiple_of(half * rows, rows), rows)


class _Exchange:
    def __init__(self, inputs, out_shape, n_sems, start, finish=None, middle=None, rest=None):
        self.inputs, self.out_shape, self.n_sems, self.start = inputs, out_shape, n_sems, start
        if finish is None:
            def finish(*refs):
                middle(*refs)
                rest(*refs)
        self.finish = finish
        self.middle = middle if middle is not None else (lambda *refs: None)
        self.rest = rest if rest is not None else finish

    def sem_shapes(self):
        return [pltpu.SemaphoreType.DMA((self.n_sems,)), pltpu.SemaphoreType.DMA((self.n_sems,))]

    def split(self, refs):
        n_in, n_out = len(self.inputs), len(self.out_shape)
        return refs[:n_in], refs[n_in:n_in + n_out], refs[n_in + n_out], refs[n_in + n_out + 1]


def _allgather_exchange(shards, whole=()):
    n, nw = len(shards), len(whole)
    slots = 8

    def plan(src, outs, send_sems, recv_sems):
        x, y, c = _place()
        via_x, via_y, diagonal = _other_chips(x, y)
        id_x, id_y, id_diagonal = [2 * chip[0] + chip[1] for chip in (via_x, via_y, diagonal)]
        me, sibling = (x, y, c), (x, y, 1 - c)

        def rows_of(a, half, quarter):
            total = src[a].shape[0]
            if quarter is None:
                return _half_rows(total, half)
            return pl.ds(pl.multiple_of(half * (total // 2) + quarter * (total // 4), total // 4), total // 4)

        def copy(a, k, chip_index, half, quarter, to, from_src=False):
            rows = rows_of(a, half, quarter)
            dst = outs[a].at[chip_index, rows]
            return pltpu.make_async_remote_copy(
                src_ref=src[a].at[rows] if from_src else dst, dst_ref=dst, send_sem=send_sems.at[slots * a + k],
                recv_sem=recv_sems.at[slots * a + k], device_id=to, device_id_type=MESH)

        def whole_copy(b, k, chip_index, to):
            return pltpu.make_async_remote_copy(
                src_ref=src[n + b], dst_ref=outs[n + b].at[chip_index], send_sem=send_sems.at[slots * n + 3 * b + k],
                recv_sem=recv_sems.at[slots * n + 3 * b + k], device_id=to, device_id_type=MESH)

        first, stages, last = [], [], []
        for a in range(n):
            first += [copy(a, 0, 2 * x + y, c, None, (*via_x, c), True), copy(a, 1, 2 * x + y, c, None, (*via_y, c), True)]
            stages.append([
                (copy(a, 0, id_x, c, None, me),
                 [copy(a, 2, id_x, c, 0, (*via_y, c)), copy(a, 4, id_x, c, None, sibling)]),
                (copy(a, 1, id_y, c, None, me),
                 [copy(a, 3, id_y, c, 1, (*via_x, c)), copy(a, 5, id_y, c, None, sibling)]),
                (copy(a, 2, id_diagonal, c, 0, me), [copy(a, 6, id_diagonal, c, 0, sibling)]),
                (copy(a, 3, id_diagonal, c, 1, me), [copy(a, 7, id_diagonal, c, 1, sibling)]),
            ])
            last += [copy(a, 4, id_x, 1 - c, None, me), copy(a, 5, id_y, 1 - c, None, me),
                     copy(a, 6, id_diagonal, 1 - c, 0, me), copy(a, 7, id_diagonal, 1 - c, 1, me)]
        for b in range(nw):
            for k, (chip, index) in enumerate(((via_x, id_x), (via_y, id_y), (diagonal, id_diagonal))):
                first.append(whole_copy(b, k, 2 * x + y, (*chip, c)))
                last.append(whole_copy(b, k, index, me))
        return first, stages, last

    def start(*refs):
        for cp in plan(*refs)[0]:
            cp.start()

    def pass_on(stages, which):
        for stage in which:
            for per_shard in stages:
                lands, onward = per_shard[stage]
                lands.wait_recv()
                for cp in onward:
                    cp.start()

    def middle(*refs):
        pass_on(plan(*refs)[1], (0, 1))

    def rest(*refs):
        first, stages, last = plan(*refs)
        pass_on(stages, (2, 3))
        for cp in last:
            cp.wait_recv()
        for cp in first + [cp for per_shard in stages for _, onward in per_shard for cp in onward]:
            cp.wait_send()

    out_shape = [jax.ShapeDtypeStruct((NCHIP,) + s.shape, s.dtype) for s in list(shards) + list(whole)]
    return _Exchange(list(shards) + list(whole), out_shape, slots * n + 3 * nw, start, middle=middle, rest=rest)


def _with_own(gathered, own):
    x, y, _ = _place()
    return lax.dynamic_update_index_in_dim(gathered, own, 2 * x + y, axis=0)


def _simple_exchange(inputs, out_shape, copies_of):
    def start(*refs):
        for cp in copies_of(*refs):
            cp.start()

    def finish(*refs):
        for cp in copies_of(*refs):
            cp.wait()

    return _Exchange(list(inputs), out_shape, len(out_shape) * 3, start, finish)


def _pair_exchange(grads):
    def copies_of(src, outs, send_sems, recv_sems):
        x, y, c = _place()
        return [pltpu.make_async_remote_copy(
            src_ref=src[a].at[:, _half_rows(src[a].shape[1], 1 - c)], dst_ref=outs[a], send_sem=send_sems.at[a],
            recv_sem=recv_sems.at[a], device_id=(x, y, 1 - c), device_id_type=MESH) for a in range(len(src))]

    return _simple_exchange(
        grads, [jax.ShapeDtypeStruct((g.shape[0], g.shape[1] // 2, g.shape[2]), g.dtype) for g in grads], copies_of)


def _pair_sum(grads, theirs, name):
    n = len(grads)

    def body(place_ref, *refs):
        for a in range(n):
            refs[2 * n + a][...] = (refs[a][...].astype(F32) + refs[n + a][...].astype(F32)).astype(BF16)

    def specs(arrs):
        return [pl.BlockSpec((None,) + g.shape[1:], lambda j, place: (j, 0, 0)) for g in arrs]

    own_half = [pl.BlockSpec((None, g.shape[1] // 2, g.shape[2]), lambda j, place: (j, place[1], 0)) for g in grads]
    return pl.pallas_call(
        body, name=name, grid_spec=pltpu.PrefetchScalarGridSpec(
            num_scalar_prefetch=1, grid=(NCHIP,), in_specs=own_half + specs(theirs), out_specs=specs(theirs)),
        out_shape=[jax.ShapeDtypeStruct(g.shape, BF16) for g in theirs], compiler_params=_cparams(),
    )(_place_scalars(), *grads, *theirs)


def _chip_exchange(parts, rows_by_core=None):
    if rows_by_core is None:
        rows_by_core = [(p.shape[1],) * 2 for p in parts]

    def each_core(act):
        def run(src, outs, send_sems, recv_sems):
            x, y, c = _place()
            for core in (0, 1):
                @pl.when(c == core)
                def _(core=core):
                    for a in range(len(src)):
                        rows = pl.ds(0, rows_by_core[a][core])
                        for k, chip in enumerate(_other_chips(x, y)):
                            act(pltpu.make_async_remote_copy(
                                src_ref=src[a].at[2 * chip[0] + chip[1], rows], dst_ref=outs[a].at[k, rows],
                                send_sem=send_sems.at[3 * a + k], recv_sem=recv_sems.at[3 * a + k],
                                device_id=(*chip, core), device_id_type=MESH))

        return run

    return _Exchange(list(parts), [jax.ShapeDtypeStruct((NCHIP - 1,) + p.shape[1:], p.dtype) for p in parts],
                     3 * len(parts), each_core(lambda cp: cp.start()), each_core(lambda cp: cp.wait()))


def _chip_sum(parts, received, exchange=None):
    n = len(parts)
    steps = 4
    ex_in, ex_in_specs, ex_out_specs, ex_out_shape, ex_scratch = _hosted(exchange)
    n_ex = len(ex_in)

    def body(place_ref, *refs):
        mine, theirs = refs[2 * n + n_ex:3 * n + n_ex], refs[3 * n + n_ex:4 * n + n_ex]
        ex_refs = refs[2 * n:2 * n + n_ex] + refs[4 * n + n_ex:len(refs) - n - 2]
        tiles, send_sems, recv_sems = refs[len(refs) - n - 2:len(refs) - 2], refs[-2], refs[-1]
        step = pl.program_id(0)
        if exchange is not None:
            @pl.when(step == 0)
            def _():
                exchange.start(*exchange.split(ex_refs))

        x, y, c = _place()

        def share(a, i):
            rows = tiles[a].shape[1]
            return pltpu.make_async_remote_copy(
                src_ref=tiles[a].at[i], dst_ref=theirs[a].at[pl.ds(pl.multiple_of(i * rows, rows), rows)],
                send_sem=send_sems.at[a * steps + i], recv_sem=recv_sems.at[a * steps + i], device_id=(x, y, 1 - c),
                device_id_type=MESH)

        for a in range(n):
            own, r = refs[a], refs[n + a]
            total = ((own[...].astype(F32) + r[0].astype(F32)) + r[1].astype(F32)) + r[2].astype(F32)
            mine[a][...] = total
            tiles[a][step] = total
            share(a, step).start()

        @pl.when(step == steps - 1)
        def _():
            if exchange is not None:
                exchange.finish(*exchange.split(ex_refs))
            for a in range(n):
                for i in range(steps):
                    share(a, i).wait()

    own_specs = [pl.BlockSpec((None, g.shape[1] // steps, g.shape[2]), lambda i, place: (place[0], i, 0)) for g in parts]
    received_specs = [pl.BlockSpec((g.shape[0], g.shape[1] // steps, g.shape[2]), lambda i, place: (0, i, 0))
                      for g in received]
    out_specs = [pl.BlockSpec((g.shape[1] // steps, g.shape[2]), lambda i, place: (i, 0)) for g in parts]
    halves = [jax.ShapeDtypeStruct(g.shape[1:], F32) for g in parts]
    res = pl.pallas_call(
        body, name="grads_chip_sum", grid_spec=pltpu.PrefetchScalarGridSpec(
            num_scalar_prefetch=1, grid=(steps,), in_specs=own_specs + received_specs + ex_in_specs,
            out_specs=out_specs + [HBM] * n + ex_out_specs,
            scratch_shapes=ex_scratch + [pltpu.VMEM((steps, g.shape[1] // steps, g.shape[2]), F32) for g in parts]
            + [pltpu.SemaphoreType.DMA((n * steps,))] * 2),
        out_shape=halves * 2 + ex_out_shape, compiler_params=_cparams(),
    )(_place_scalars(), *parts, *received, *ex_in)
    return res[:n], res[n:2 * n], res[2 * n:]


def _adamw_math(w, g, m, v):
    nm = ADAM_B1 * m + (1.0 - ADAM_B1) * g
    nv = ADAM_B2 * v + (1.0 - ADAM_B2) * jnp.square(g)
    m_hat = nm / (1.0 - ADAM_B1 ** ADAM_STEP)
    v_hat = nv / (1.0 - ADAM_B2 ** ADAM_STEP)
    return -ADAM_LR * (m_hat / (jnp.sqrt(v_hat) + ADAM_EPS) + ADAM_WD * w), nm, nv


def _adamw_big(ws, g_mine, g_theirs, ms, vs):
    n = len(ws)
    steps = 8
    per_half = steps // 2

    def body(place_ref, *refs):
        outs = refs[5 * n:]
        own_half = (pl.program_id(0) // per_half) == place_ref[1]
        for a in range(n):
            g = jnp.where(own_half, refs[n + a][...], refs[2 * n + a][...])
            d, nm, nv = _adamw_math(refs[a][...], g, refs[3 * n + a][...], refs[4 * n + a][...])
            outs[a][...] = g
            outs[n + a][...] = d
            outs[2 * n + a][...] = nm
            outs[3 * n + a][...] = nv

    specs = [pl.BlockSpec((w.shape[0] // steps, w.shape[1]), lambda i, place: (i, 0)) for w in ws]

    def half_specs(halves, of_this_core):
        def tile(i, place):
            first = (place[1] if of_this_core else 1 - place[1]) * per_half
            return jnp.clip(i - first, 0, per_half - 1), 0

        return [pl.BlockSpec((g.shape[0] // per_half, g.shape[1]), tile) for g in halves]

    shapes = [jax.ShapeDtypeStruct(w.shape, F32) for w in ws]
    res = pl.pallas_call(
        body, name="adamw_big", grid_spec=pltpu.PrefetchScalarGridSpec(
            num_scalar_prefetch=1, grid=(steps,),
            in_specs=specs + half_specs(g_mine, True) + half_specs(g_theirs, False) + specs * 2, out_specs=specs * 4),
        out_shape=shapes * 4, compiler_params=_cparams(),
    )(_place_scalars(), *ws, *g_mine, *g_theirs, *ms, *vs)
    return res[:n], res[n:2 * n], res[2 * n:3 * n], res[3 * n:]


def _adamw_in(w, g_mine, g_theirs, m, v):
    half = D // 2

    def body(w_ref, gm_ref, gt_ref, m_ref, v_ref, g_out, d_out, nm_out, nv_out, g_ref):
        south = lax.axis_index("c") == 0
        g_ref[0:half, :] = jnp.where(south, gm_ref[...], gt_ref[...])
        g_ref[half:D, :] = jnp.where(south, gt_ref[...], gm_ref[...])
        g = g_ref[0:CW, :]
        d, nm, nv = _adamw_math(w_ref[...], g, m_ref[...], v_ref[...])
        g_out[...] = g
        d_out[...] = d
        nm_out[...] = nm
        nv_out[...] = nv

    cols = 2 * LANES
    spec = pl.BlockSpec((CW, cols), lambda i: (0, i))
    half_spec = pl.BlockSpec((half, cols), lambda i: (0, i))
    return pl.pallas_call(
        body, name="adamw_in", grid=(D // cols,), in_specs=[spec, half_spec, half_spec, spec, spec], out_specs=[spec] * 4,
        out_shape=[jax.ShapeDtypeStruct((CW, D), F32)] * 4, scratch_shapes=[pltpu.VMEM((D, cols), F32)],
        compiler_params=_cparams(),
    )(w, g_mine, g_theirs, m, v)


NORM_NAMES = ("pre_mix_norm", "post_mix_norm", "pre_mlp_norm", "post_mlp_norm")
SMALL_NAMES = NORM_NAMES + ("gdn_conv_w", "fox_f_bias", "gdn_dt_bias", "gdn_a_log", "fox_out_norm", "gdn_out_norm")
CONV_COLS = 3 * DGDN // NCHIP


def _small_gather(d_norms, d_conv, sums, d_fox_norm, d_gdn_norm, loss_row):
    n_arrays = 6
    n_remote = n_arrays * (NDEV - 1)

    def copies_of(src, outs, send_sems, recv_sems):
        x, y, c = _place()
        me = 4 * x + 2 * y + c

        def from_me(chip_index):
            cols = pl.ds(pl.multiple_of(chip_index * CONV_COLS, LANES), CONV_COLS)
            return [src[0], src[1].at[:, cols], src[2], src[3], src[4], src[5]]

        local = [pltpu.make_async_copy(s, outs[a].at[me], send_sems.at[n_remote + a]) for a, s in enumerate(from_me(2 * x + y))]
        remote = []
        for k in range(1, NDEV):
            px, py, pc = x ^ ((k >> 2) & 1), y ^ ((k >> 1) & 1), c ^ (k & 1)
            remote += [pltpu.make_async_remote_copy(
                src_ref=s, dst_ref=outs[a].at[me], send_sem=send_sems.at[n_arrays * (k - 1) + a],
                recv_sem=recv_sems.at[n_arrays * (k - 1) + a], device_id=(px, py, pc), device_id_type=MESH)
                for a, s in enumerate(from_me(2 * px + py))]
        return local + remote

    def start(*refs):
        for cp in copies_of(*refs):
            cp.start()

    def finish(*refs):
        for cp in copies_of(*refs):
            cp.wait()

    shapes = [(4, D), (CONV_K, CONV_COLS), (8, LANES), (1, LANES), (1, LANES), (1, LANES)]
    return _Exchange([d_norms, d_conv, sums, d_fox_norm, d_gdn_norm, loss_row],
                     [jax.ShapeDtypeStruct((NDEV,) + s, F32) for s in shapes], n_remote + n_arrays, start, finish)


def _small_adamw(gathered, ws, ms, vs):
    n = len(SMALL_NAMES)
    ng = len(gathered)

    def body(*refs):
        def total(buf):
            acc = buf[0]
            for i in range(1, NDEV):
                acc = acc + buf[i]
            return acc

        t_norms, t_conv, t_sums, t_fn, t_gn, t_loss = [total(r) for r in refs[:ng]]
        w_refs, m_refs, v_refs = refs[ng:ng + n], refs[ng + n:ng + 2 * n], refs[ng + 2 * n:ng + 3 * n]
        outs = refs[ng + 3 * n:]
        outs[4 * n][...] = t_loss
        grads = [t_norms[i:i + 1, :] for i in range(4)] + [
            t_conv, t_sums[0:1, 0:NFH], t_sums[1:2, 0:NGH], t_sums[2:3, 0:NGH], t_fn[:, 0:FHD], t_gn]
        for a in range(n):
            d, nm, nv = _adamw_math(w_refs[a][...], grads[a], m_refs[a][...], v_refs[a][...])
            outs[a][...] = grads[a]
            outs[n + a][...] = d
            outs[2 * n + a][...] = nm
            outs[3 * n + a][...] = nv

    def whole(arr):
        return pl.BlockSpec(arr.shape, lambda i: (0,) * arr.ndim)

    res = pl.pallas_call(
        body, name="small_adamw", grid=(1,), in_specs=[whole(t) for t in gathered] + [whole(w) for w in ws] * 3,
        out_specs=[whole(w) for w in ws] * 4 + [pl.BlockSpec((1, LANES), lambda i: (0, 0))],
        out_shape=[jax.ShapeDtypeStruct(w.shape, F32) for w in ws] * 4 + [jax.ShapeDtypeStruct((1, LANES), F32)],
        compiler_params=_cparams(),
    )(*gathered, *ws, *ms, *vs)
    return res[:n], res[n:2 * n], res[2 * n:3 * n], res[3 * n:4 * n], res[4 * n]


CW = DPROJ // NCHIP
BF16_ROWS = 16
CW_SENT = -(-CW // BF16_ROWS) * BF16_ROWS
PROJ_RUNS = tuple((part * DFOX + hp * LANES, part * DFOX + (hp + 1) * LANES, (3 * hp + part) * LANES)
                  for hp in range(NPAIR) for part in range(3)) + (
    (1536, 1544, BLK_SMALL * LANES), (1544, 3080, BLK_GDN * LANES), (3080, 3088, BLK_SMALL * LANES + 8),
    (3088, 3600, BLK_GZ * LANES))


def _proj_pieces():
    pieces = []
    for lo, hi, at in PROJ_RUNS:
        while lo < hi:
            j = lo // CW
            end = min(hi, (j + 1) * CW)
            pieces.append((j, lo - j * CW, at, end - lo))
            at, lo = at + end - lo, end
    return pieces


RT = 256


def _to_padded_rows(gathered):
    def body(src_ref, out_ref, blocks_ref, rows_ref):
        blocks_ref[...] = src_ref[...].astype(F32)
        rows_ref[...] = jnp.zeros_like(rows_ref)
        for j, start, at, n in _proj_pieces():
            rows_ref[at:at + n, :] = blocks_ref[j, start:start + n, :]
        out_ref[...] = rows_ref[...].astype(out_ref.dtype)

    return pl.pallas_call(
        body, name="proj_rows_in", grid=(D // RT,), in_specs=[pl.BlockSpec((NCHIP, D, RT), lambda i: (0, 0, i))],
        out_specs=pl.BlockSpec((DPROJ_PAD, RT), lambda i: (0, i)), out_shape=jax.ShapeDtypeStruct((DPROJ_PAD, D), gathered.dtype),
        scratch_shapes=[pltpu.VMEM((NCHIP, D, RT), F32), pltpu.VMEM((DPROJ_PAD, RT), F32)], compiler_params=_cparams(),
    )(gathered)


def _from_padded_rows_pair_sum(w):
    steps = D // RT
    half = D // 2

    def body(src_ref, out_ref, rows_ref, blocks_ref, mine_ref, send_ref, recv_ref, send_sems, recv_sems):
        i = pl.program_id(0)
        x, y, c = _place()

        def share(t):
            return pltpu.make_async_remote_copy(
                src_ref=send_ref.at[t], dst_ref=recv_ref.at[t], send_sem=send_sems.at[t], recv_sem=recv_sems.at[t],
                device_id=(x, y, 1 - c), device_id_type=MESH)

        def rows_of(core):
            return blocks_ref[:, pl.ds(pl.multiple_of(core * half, half), half), :]

        @pl.when(i < steps)
        def _():
            rows_ref[...] = src_ref[...].astype(F32)
            blocks_ref[...] = jnp.zeros_like(blocks_ref)
            for j, start, at, n in _proj_pieces():
                blocks_ref[j, start:start + n, :] = rows_ref[at:at + n, :]
            mine_ref[i % 2] = rows_of(c)
            send_ref[i] = rows_of(1 - c).astype(BF16)
            share(i).start()

        @pl.when(i > 0)
        def _():
            share(i - 1).wait_recv()
            out_ref[...] = (mine_ref[(i - 1) % 2] + recv_ref[i - 1].astype(F32)).astype(BF16)

        @pl.when(i == steps)
        def _():
            for t in range(steps):
                share(t).wait_send()

    tile = (NCHIP, half, RT)
    return pl.pallas_call(
        body, name="proj_rows_out_pair_sum", grid=(steps + 1,),
        in_specs=[pl.BlockSpec((DPROJ_PAD, RT), lambda i: (0, jnp.minimum(i, steps - 1)))],
        out_specs=pl.BlockSpec(tile, lambda i: (0, 0, jnp.maximum(i - 1, 0))),
        out_shape=jax.ShapeDtypeStruct((NCHIP, half, D), BF16),
        scratch_shapes=[pltpu.VMEM((DPROJ_PAD, RT), F32), pltpu.VMEM((NCHIP, D, RT), F32), pltpu.VMEM((2,) + tile, F32),
                        pltpu.VMEM((steps,) + tile, BF16), pltpu.VMEM((steps,) + tile, BF16),
                        pltpu.SemaphoreType.DMA((steps,)), pltpu.SemaphoreType.DMA((steps,))],
        compiler_params=_cparams(),
    )(w)


def _local_step(x, target, first_weights, late_weights, reduce_late, reduce_in, pre_mix_norm, fox_f_bias, fox_out_norm,
                gdn_a_log, gdn_dt_bias, gdn_out_norm, post_mix_norm, pre_mlp_norm, post_mlp_norm):
    bias_vec = jnp.zeros((1, LANES), F32).at[0, 0:NFH].set(fox_f_bias).at[0, LANE_G:LANE_G + NGH].set(gdn_dt_bias)
    alog_vec = jnp.zeros((1, LANES), F32).at[0, LANE_G:LANE_G + NGH].set(gdn_a_log)
    w2 = jnp.concatenate([fox_out_norm, fox_out_norm], axis=1)

    h, first = _pre_norm(x, pre_mix_norm, exchange=first_weights[0])
    win_p, conv_w = first_weights[1](first)
    proj = _matmul(h, win_p, tb=True, tm=2048, tn=768, tk=1024, name="mm_proj", exchange=late_weights[0])
    proj, late_a = proj if late_weights[0] is not None else (proj, [])
    gates = _gates(proj, bias_vec, alog_vec)
    mix, fox_o, lse, late_b = _fox_fwd(proj, gates, w2, exchange=late_weights[1])
    qkv = _gdn_pre(proj, conv_w)
    (u, w, qd, kd, a_intra, gl, t_inv), _ = _gdn_prep(qkv, gates)
    wout, wup3 = late_weights[3](late_a, late_b)
    mix, gdn_raw, states = _gdn_scan(u, w, qd, kd, a_intra, gl, proj, gdn_out_norm, mix)
    def post_mix(acc, xv, w_post, w_pre_mlp):
        x1v = xv + acc * _rms_scale(acc) * w_post
        return acc, x1v, x1v * _rms_scale(x1v) * w_pre_mlp

    mixed, x1, h2 = _matmul(mix, wout, tm=512, tn=D, tk=1024, out_dtypes=(F32, F32, BF16), name="mm_out",
                            extra=(x, post_mix_norm, pre_mlp_norm), epilogue=post_mix)

    def relu2(acc):
        r = jnp.maximum(acc, 0.0)
        return r, r * r

    up_act = _matmul(h2, wup3, b3=True, tm=1024, tn=1024, tk=1024, out_dtypes=(BF16, BF16), epilogue=relu2,
                     name="mm_up", exchange=late_weights[2])
    (up_relu, act), late_c = up_act if late_weights[2] is not None else (up_act, [])
    wdown = late_weights[4](late_c)
    def loss_head(acc, x1v, tv, w):
        err = x1v + acc * _rms_scale(acc) * w - tv
        dx2v = err * (1.0 / D)
        dyv, dwt = _rms_bwd(acc, w, dx2v)
        part = 0.5 * jnp.sum(jnp.mean(err * err, axis=-1, keepdims=True), axis=0, keepdims=True)
        return dx2v, dyv, jnp.sum(dwt, axis=0, keepdims=True), jnp.broadcast_to(part, (1, D))

    dx2, dy, d_post_mlp, loss_wide = _matmul(
        act, wdown, tm=512, tn=D, tk=DFF, out_dtypes=(F32, BF16, F32, F32), extra=(x1, target, post_mlp_norm),
        epilogue=loss_head, n_sums=2, name="mm_down")
    loss_row = loss_wide[:, :LANES]

    dwdown = _matmul(act, dy, ta=True, tm=1024, tn=1024, tk=2048, out_dtypes=(BF16,), name="mm_dwdown")

    def relu2_bwd(acc, r):
        return (acc * 2.0 * r.astype(F32),)

    dup = _matmul(dy, wdown, tb=True, tm=1024, tn=1024, tk=1024, out_dtypes=(BF16,), extra=(up_relu,), epilogue=relu2_bwd,
                  name="mm_dact")
    dwup3 = _matmul(h2, dup, ta=True, tm=1024, tn=1024, tk=2048, out_dtypes=(BF16,), o3=True, name="mm_dwup")
    def mid_bwd(acc, x1v, dx2v, mixedv, w_pre_mlp, w_post):
        dxa, dwm = _rms_bwd(x1v, w_pre_mlp, acc)
        dx1v = dx2v + dxa
        dm, dwp = _rms_bwd(mixedv, w_post, dx1v)
        return dx1v, dm, jnp.sum(dwm, axis=0, keepdims=True), jnp.sum(dwp, axis=0, keepdims=True)

    dx1, dmixed, d_pre_mlp, d_post_mix = _matmul(
        dup, wup3, tb=True, b3=True, tm=512, tn=D, tk=DFF, out_dtypes=(F32, BF16, F32, F32),
        extra=(x1, dx2, mixed, pre_mlp_norm, post_mix_norm), epilogue=mid_bwd, n_sums=2, name="mm_dh2")
    dwout = _matmul(mix, dmixed, ta=True, tm=256, tn=1024, tk=2048, out_dtypes=(BF16,), name="mm_dwout")
    dmix = _matmul(dmixed, wout, tb=True, tm=512, tn=1024, tk=1024, name="mm_dmix")

    dfox, delta, d_fox_norm, from_sibling = _fox_norm_bwd(fox_o, dmix, w2, exchange=reduce_late[0](dwout, dwup3, dwdown))
    dproj, dcum_fox, reduced_a = _fox_bwd(proj, dfox, gates, lse, delta, exchange=reduce_late[1](from_sibling))
    (dproj, du, dw, dqd, dkd, da, dgl, d_gdn_norm), reduced_b = _gdn_scan_bwd(
        dmix, gdn_raw, proj, gdn_out_norm, u, w, qd, kd, a_intra, gl, states, dproj, exchange=reduce_late[2]())
    dqkv, dgates_gdn, reduced_c = _gdn_prep_bwd(qkv, gates, t_inv, du, dw, dqd, dkd, da, dgl, exchange=reduce_late[3]())
    reduced_late = (reduced_a, reduced_b, reduced_c)
    dproj, d_conv = _gdn_pre_bwd(proj, conv_w, dqkv, dproj)
    dproj, sums = _gates_bwd(proj, bias_vec, alog_vec, dgates_gdn, dcum_fox, dproj)

    dwin_p = _matmul(dproj, h, ta=True, tm=1280, tn=1024, tk=2048, out_dtypes=(BF16,), name="mm_dwin")
    exchange_in = reduce_in(dwin_p)
    dh = _matmul(dproj, win_p, tm=1024, tk=DPROJ_PAD, name="mm_dh", exchange=exchange_in)
    dh, reduced_in = dh if exchange_in is not None else (dh, [])
    grad_x, d_pre_mix = _pre_norm_bwd(dh, x, pre_mix_norm, dx1)

    d_norms = jnp.concatenate([d_pre_mix, d_post_mix, d_pre_mlp, d_post_mlp], axis=0)
    return grad_x, (d_norms, d_conv, sums, d_fox_norm, d_gdn_norm, loss_row), reduced_late, reduced_in


def kernel(x, pre_mix_norm, w_in, fox_f_bias, fox_out_norm, gdn_conv_w, gdn_a_log, gdn_dt_bias, gdn_out_norm, w_out, post_mix_norm, pre_mlp_norm, w_up, w_down, post_mlp_norm, loss_target, m_pre_mix_norm, m_w_in, m_fox_f_bias, m_fox_out_norm, m_gdn_conv_w, m_gdn_a_log, m_gdn_dt_bias, m_gdn_out_norm, m_w_out, m_post_mix_norm, m_pre_mlp_norm, m_w_up, m_w_down, m_post_mlp_norm, v_pre_mix_norm, v_w_in, v_fox_f_bias, v_fox_out_norm, v_gdn_conv_w, v_gdn_a_log, v_gdn_dt_bias, v_gdn_out_norm, v_w_out, v_post_mix_norm, v_pre_mlp_norm, v_w_up, v_w_down, v_post_mlp_norm):
    weights = dict(pre_mix_norm=pre_mix_norm, w_in=w_in, fox_f_bias=fox_f_bias, fox_out_norm=fox_out_norm, gdn_conv_w=gdn_conv_w,
                   gdn_a_log=gdn_a_log, gdn_dt_bias=gdn_dt_bias, gdn_out_norm=gdn_out_norm, w_out=w_out, post_mix_norm=post_mix_norm,
                   pre_mlp_norm=pre_mlp_norm, w_up=w_up, w_down=w_down, post_mlp_norm=post_mlp_norm)
    m_in = dict(pre_mix_norm=m_pre_mix_norm, w_in=m_w_in, fox_f_bias=m_fox_f_bias, fox_out_norm=m_fox_out_norm, gdn_conv_w=m_gdn_conv_w,
                gdn_a_log=m_gdn_a_log, gdn_dt_bias=m_gdn_dt_bias, gdn_out_norm=m_gdn_out_norm, w_out=m_w_out, post_mix_norm=m_post_mix_norm,
                pre_mlp_norm=m_pre_mlp_norm, w_up=m_w_up, w_down=m_w_down, post_mlp_norm=m_post_mlp_norm)
    v_in = dict(pre_mix_norm=v_pre_mix_norm, w_in=v_w_in, fox_f_bias=v_fox_f_bias, fox_out_norm=v_fox_out_norm, gdn_conv_w=v_gdn_conv_w,
                gdn_a_log=v_gdn_a_log, gdn_dt_bias=v_gdn_dt_bias, gdn_out_norm=v_gdn_out_norm, w_out=v_w_out, post_mix_norm=v_post_mix_norm,
                pre_mlp_norm=v_pre_mlp_norm, w_up=v_w_up, w_down=v_w_down, post_mlp_norm=v_post_mlp_norm)
    order_w = ("pre_mix_norm", "w_in", "fox_f_bias", "fox_out_norm", "gdn_conv_w", "gdn_a_log", "gdn_dt_bias", "gdn_out_norm", "w_out",
               "post_mix_norm", "pre_mlp_norm", "w_up", "w_down", "post_mlp_norm")
    big = ("w_in", "w_out", "w_up", "w_down")

    def row(v):
        return v if v.ndim == 2 else v.reshape(1, -1)

    win_shard = jnp.pad(w_in.T.astype(BF16), ((0, D - CW), (0, 0)))

    def resolve_first(gathered):
        win_g, conv_g = gathered
        return (_to_padded_rows(_with_own(win_g, win_shard)),
                _with_own(conv_g, gdn_conv_w).transpose(1, 0, 2).reshape(CONV_K, 3 * DGDN))

    late_shards = [weights[n].astype(BF16) for n in big[1:]]

    gathered_down = []

    def resolve_out_up(gathered_out, gathered_mlp):
        gathered_down.append(gathered_mlp[1])
        return _with_own(gathered_out[0], late_shards[0]).reshape(D, D), _with_own(gathered_mlp[0], late_shards[1])

    def resolve_down(_):
        return _with_own(gathered_down[0], late_shards[2]).reshape(DFF, D)

    pair_sums, late_blocks = {}, []

    def pair_summed(names, blocks, theirs):
        for n, s in zip(names, _pair_sum(blocks, theirs, "grads_pair_sum_" + names[0])):
            pair_sums[n] = s

    def late_pair_exchange(dwout, dwup3, dwdown):
        late_blocks.extend([dwout.reshape(NCHIP, D // NCHIP, D), dwup3, dwdown.reshape(NCHIP, DFF // NCHIP, D)])
        return _pair_exchange(late_blocks)

    def late_chip_exchange(theirs):
        pair_summed(big[1:], late_blocks, theirs)
        return _chip_exchange([pair_sums["w_up"], pair_sums["w_down"]])

    def reduce_in(dwin_p):
        pair_sums["w_in"] = _from_padded_rows_pair_sum(dwin_p)
        return _chip_exchange([pair_sums["w_in"]], rows_by_core=[(D // 2, CW_SENT - D // 2)])

    grad_x, small, received_late, received_in = _local_step(
        x[0], loss_target[0], (_allgather_exchange([win_shard], whole=[gdn_conv_w]), resolve_first),
        (_allgather_exchange(late_shards[:1]), _allgather_exchange(late_shards[1:]), None, resolve_out_up, resolve_down),
        (late_pair_exchange, late_chip_exchange, lambda: None, lambda: _chip_exchange([pair_sums["w_out"]])),
        reduce_in, row(pre_mix_norm), fox_f_bias, row(fox_out_norm), gdn_a_log, gdn_dt_bias,
        row(gdn_out_norm), row(post_mix_norm), row(pre_mlp_norm), row(post_mlp_norm))
    received_mlp, _, received_out = received_late

    g_mine, g_theirs, small_gathered = _chip_sum(
        [pair_sums[n] for n in big], list(received_in[:1]) + list(received_out[:1]) + list(received_mlp[:2]),
        exchange=_small_gather(*small))

    g_big, d_big, nm_big, nv_big = _adamw_big(
        [weights[n] for n in big[1:]], g_mine[1:], g_theirs[1:], [m_in[n] for n in big[1:]], [v_in[n] for n in big[1:]])
    in_t = _adamw_in(w_in.T, g_mine[0], g_theirs[0], m_w_in.T, v_w_in.T)
    g_small, d_small, nm_small, nv_small, loss_total = _small_adamw(
        small_gathered, [row(weights[n]) for n in SMALL_NAMES], [row(m_in[n]) for n in SMALL_NAMES],
        [row(v_in[n]) for n in SMALL_NAMES])

    grads, delta, new_m, new_v = {}, {}, {}, {}
    grads["w_in"], delta["w_in"], new_m["w_in"], new_v["w_in"] = [t.T for t in in_t]
    for i, n in enumerate(big[1:]):
        grads[n], delta[n], new_m[n], new_v[n] = g_big[i], d_big[i], nm_big[i], nv_big[i]
    for i, n in enumerate(SMALL_NAMES):
        shape = weights[n].shape
        grads[n], delta[n], new_m[n], new_v[n] = (g_small[i].reshape(shape), d_small[i].reshape(shape),
                                                  nm_small[i].reshape(shape), nv_small[i].reshape(shape))
    return (loss_total[0, 0], grad_x[None], *[grads[n] for n in order_w], *[delta[n] for n in order_w], *[new_m[n] for n in order_w],
            *[new_v[n] for n in order_w])
```

```python
import jax
import jax.numpy as jnp
from jax import lax
from jax.experimental import pallas as pl
from jax.experimental.pallas import tpu as pltpu

F32 = jnp.float32
BF16 = jnp.bfloat16
MESH = pl.DeviceIdType.MESH

S = 2048
D = 1024
NFH, FHD = 8, 64
NPAIR = NFH // 2
NGH, GHD = 4, 128
DFOX = NFH * FHD
DGDN = NGH * GHD
CHUNK = 64
NCH = S // CHUNK
CONV_K = 4
DFF = 4 * D
EPS = 1e-6
DPROJ = 3600
LANES = 128
DPROJ_PAD = 3840
BLK_GDN = 12
BLK_GZ = 24
BLK_SMALL = 28
NCHIP = 4
NDEV = 8
VMEM_LIMIT = 56 * 1024 * 1024

ADAM_LR = 0.001
ADAM_B1 = 0.9
ADAM_B2 = 0.999
ADAM_EPS = 1e-08
ADAM_WD = 0.01
ADAM_STEP = 10


def _cparams(**kw):
    return pltpu.CompilerParams(vmem_limit_bytes=VMEM_LIMIT, **kw)


def _dn(ca, cb):
    return (((ca,), (cb,)), ((), ()))


def _dot(a, b, ca=1, cb=0):
    return lax.dot_general(a.astype(BF16), b.astype(BF16), _dn(ca, cb), preferred_element_type=F32)


def _hdot(a, b, ca=1, cb=0):
    return lax.dot_general(a.astype(F32), b.astype(F32), _dn(ca, cb), precision=lax.Precision.HIGHEST,
                           preferred_element_type=F32)


def _dot3(a, b, ca=1, cb=0):
    a_hi, b_hi = a.astype(BF16), b.astype(BF16)
    a_lo, b_lo = (a - a_hi.astype(F32)).astype(BF16), (b - b_hi.astype(F32)).astype(BF16)
    dn = _dn(ca, cb)
    return (lax.dot_general(a_hi, b_hi, dn, preferred_element_type=F32)
            + (lax.dot_general(a_hi, b_lo, dn, preferred_element_type=F32)
               + lax.dot_general(a_lo, b_hi, dn, preferred_element_type=F32)))


@jax.custom_vjp
def _mm_nn(a, b):
    return _dot(a, b, 1, 0)


def _mm_nn_fwd(a, b):
    return _dot(a, b, 1, 0), (a, b)


def _mm_nn_bwd(res, g):
    a, b = res
    return _dot(g, b, 1, 1), _dot(a, g, 0, 0)


_mm_nn.defvjp(_mm_nn_fwd, _mm_nn_bwd)


@jax.custom_vjp
def _mm_nt(a, b):
    return _dot(a, b, 1, 1)


def _mm_nt_fwd(a, b):
    return _dot(a, b, 1, 1), (a, b)


def _mm_nt_bwd(res, g):
    a, b = res
    return _dot(g, b, 1, 0), _dot(g, a, 0, 0)


_mm_nt.defvjp(_mm_nt_fwd, _mm_nt_bwd)


@jax.custom_vjp
def _saved_inverse(m, t_inv):
    del m
    return t_inv


def _saved_inverse_fwd(m, t_inv):
    del m
    return t_inv, t_inv


def _saved_inverse_bwd(t_inv, g):
    return -_dot3(_dot3(t_inv, g, 0, 0), t_inv, 1, 1), jnp.zeros_like(t_inv)


_saved_inverse.defvjp(_saved_inverse_fwd, _saved_inverse_bwd)


def _sigmoid(z):
    return 1.0 / (1.0 + jnp.exp(-z))


def _softplus(z):
    return jnp.maximum(z, 0.0) + jnp.log(1.0 + jnp.exp(-jnp.abs(z)))


def _silu(z):
    return z * _sigmoid(z)


def _rms_scale(x):
    return lax.rsqrt(jnp.mean(x * x, axis=-1, keepdims=True) + EPS)


def _rms_bwd(x, w, g):
    r = _rms_scale(x)
    gw = g * w
    dx = r * gw - x * (r * r * r) * jnp.mean(gw * x, axis=-1, keepdims=True)
    return dx, g * x * r


def _matmul(a, b, *, name, ta=False, tb=False, tm=512, tn=512, tk=512, out_dtypes=(F32,), b3=False, o3=False,
            extra=(), epilogue=None, exchange=None, n_sums=0):
    m, k = (a.shape[1], a.shape[0]) if ta else a.shape
    if b3:
        n = b.shape[1] if tb else b.shape[0] * b.shape[2]
        kb = b.shape[0] * b.shape[2] if tb else b.shape[1]
    else:
        n, kb = (b.shape[0], b.shape[1]) if tb else (b.shape[1], b.shape[0])
    assert kb == k, (name, kb, k)
    tm, tn, tk = min(tm, m), min(tn, n), min(tk, k)
    assert m % tm == 0 and n % tn == 0 and k % tk == 0, (name, m, n, k, tm, tn, tk)
    nk = k // tk
    whole_k_blocks = b3 and tb and not ta and nk == 1 and b.shape[0] > 1
    n_extra = len(extra)
    n_out = len(out_dtypes)
    grid = (m // tm, n // tn, nk)
    ex_in, ex_in_specs, ex_out_specs, ex_out_shape, ex_scratch = _hosted(exchange)

    def body(*refs):
        a_ref, b_ref = refs[0], refs[1]
        extra_refs = refs[2:2 + n_extra]
        first_out = 2 + n_extra + len(ex_in)
        out_refs = refs[first_out:first_out + n_out]
        ex_refs = refs[2 + n_extra:first_out] + refs[first_out + n_out:first_out + n_out + len(ex_out_shape)] + refs[-2:]
        step = [pl.program_id(d) for d in range(3)]

        if exchange is not None:
            @pl.when((step[0] == 0) & (step[1] == 0) & (step[2] == 0))
            def _():
                exchange.start(*exchange.split(ex_refs))

        def finish(acc):
            outs = (acc,) if epilogue is None else epilogue(acc, *[r[...] for r in extra_refs])
            for o_ref, val in zip(out_refs[:n_out - n_sums], outs):
                o_ref[...] = val.astype(o_ref.dtype)
            for o_ref, val in zip(out_refs[n_out - n_sums:], outs[n_out - n_sums:]):
                @pl.when(step[0] == 0)
                def _(o_ref=o_ref, val=val):
                    o_ref[...] = val

                @pl.when(step[0] > 0)
                def _(o_ref=o_ref, val=val):
                    o_ref[...] += val

        if whole_k_blocks:
            width = b.shape[2]
            part = _dot(a_ref[:, 0:width], b_ref[0], 1, 1)
            for blk in range(1, b.shape[0]):
                part = part + _dot(a_ref[:, blk * width:(blk + 1) * width], b_ref[blk], 1, 1)
        else:
            part = _dot(a_ref[...], b_ref[...], 0 if ta else 1, 1 if tb else 0)
        if nk == 1:
            finish(part)
        else:
            acc_ref = refs[first_out + n_out + len(ex_out_shape)]

            @pl.when(step[2] == 0)
            def _():
                acc_ref[...] = part

            @pl.when(step[2] > 0)
            def _():
                acc_ref[...] += part

            @pl.when(step[2] == nk - 1)
            def _():
                finish(acc_ref[...])

        if exchange is not None:
            flat = (step[0] * grid[1] + step[1]) * nk + step[2]
            total = grid[0] * grid[1] * nk

            @pl.when(flat == total // 2)
            def _():
                exchange.middle(*exchange.split(ex_refs))

            @pl.when(flat == total - 1)
            def _():
                exchange.rest(*exchange.split(ex_refs))

    a_spec = pl.BlockSpec((tk, tm), lambda i, j, kk: (kk, i)) if ta else pl.BlockSpec((tm, tk), lambda i, j, kk: (i, kk))
    if whole_k_blocks:
        b_spec = pl.BlockSpec((b.shape[0], tn, b.shape[2]), lambda i, j, kk: (0, j, 0))
    elif b3 and tb:
        assert b.shape[2] == tk
        b_spec = pl.BlockSpec((None, tn, tk), lambda i, j, kk: (kk, j, 0))
    elif b3:
        assert b.shape[2] == tn
        b_spec = pl.BlockSpec((None, tk, tn), lambda i, j, kk: (j, kk, 0))
    elif tb:
        b_spec = pl.BlockSpec((tn, tk), lambda i, j, kk: (j, kk))
    else:
        b_spec = pl.BlockSpec((tk, tn), lambda i, j, kk: (kk, j))
    tile = pl.BlockSpec((tm, tn), lambda i, j, kk: (i, j))
    out_specs = [tile] * n_out
    out_shape = [jax.ShapeDtypeStruct((m, n), dt) for dt in out_dtypes]
    if o3:
        out_specs[0] = pl.BlockSpec((None, tm, tn), lambda i, j, kk: (j, i, 0))
        out_shape[0] = jax.ShapeDtypeStruct((n // tn, m, tn), out_dtypes[0])
    assert n_sums == 0 or tn == n
    for r in range(n_out - n_sums, n_out):
        out_specs[r] = pl.BlockSpec((1, tn), lambda i, j, kk: (0, 0))
        out_shape[r] = jax.ShapeDtypeStruct((1, n), out_dtypes[r])
    res = pl.pallas_call(
        body, name=name, grid=grid,
        in_specs=[a_spec, b_spec] + [tile if e.shape[0] == m else pl.BlockSpec((1, tn), lambda i, j, kk: (0, j)) for e in extra]
        + ex_in_specs, out_specs=out_specs + ex_out_specs,
        out_shape=out_shape + ex_out_shape,
        scratch_shapes=([pltpu.VMEM((tm, tn), F32)] if nk > 1 else []) + ex_scratch,
        compiler_params=_cparams(),
    )(a, b, *extra, *ex_in)
    if exchange is not None:
        return (res[0] if n_out == 1 else res[:n_out]), res[n_out:]
    return res[0] if n_out == 1 else res


TR = 512


def _row_spec(cols):
    return pl.BlockSpec((TR, cols), lambda i: (i, 0))


def _vec_spec(cols):
    return pl.BlockSpec((1, cols), lambda i: (0, 0))


def _pre_norm(x, w, exchange=None):
    ex_in, ex_in_specs, ex_out_specs, ex_out_shape, ex_scratch = _hosted(exchange)

    def body(*refs):
        x_ref, w_ref, h_ref = refs[0], refs[1], refs[2 + len(ex_in)]
        ex_refs = refs[2:2 + len(ex_in)] + refs[3 + len(ex_in):]
        if exchange is not None:
            @pl.when(pl.program_id(0) == 0)
            def _():
                exchange.start(*exchange.split(ex_refs))

        xv = x_ref[...]
        h_ref[...] = (xv * _rms_scale(xv) * w_ref[...]).astype(BF16)

        if exchange is not None:
            @pl.when(pl.program_id(0) == S // TR - 1)
            def _():
                exchange.finish(*exchange.split(ex_refs))

    res = pl.pallas_call(
        body, name="pre_norm", grid=(S // TR,), in_specs=[_row_spec(D), _vec_spec(D)] + ex_in_specs,
        out_specs=[_row_spec(D)] + ex_out_specs, out_shape=[jax.ShapeDtypeStruct((S, D), BF16)] + ex_out_shape,
        scratch_shapes=ex_scratch, compiler_params=_cparams(),
    )(x, w, *ex_in)
    return res[0], res[1:]


def _pre_norm_bwd(dh, x, w, dx1):
    def body(dh_ref, x_ref, w_ref, dx1_ref, dx_ref, dw_ref):
        i = pl.program_id(0)
        dxa, dwt = _rms_bwd(x_ref[...], w_ref[...], dh_ref[...])
        dx_ref[...] = dx1_ref[...] + dxa

        @pl.when(i == 0)
        def _():
            dw_ref[...] = jnp.zeros_like(dw_ref)

        dw_ref[...] += jnp.sum(dwt, axis=0, keepdims=True)

    return pl.pallas_call(
        body, name="pre_norm_bwd", grid=(S // TR,),
        in_specs=[_row_spec(D), _row_spec(D), _vec_spec(D), _row_spec(D)], out_specs=[_row_spec(D), _vec_spec(D)],
        out_shape=[jax.ShapeDtypeStruct((S, D), F32), jax.ShapeDtypeStruct((1, D), F32)], compiler_params=_cparams(),
    )(dh, x, w, dx1)


BQ = 512
NQ = S // BQ
LANE_BETA, LANE_G = 8, 12


def _gate_lanes(shape):
    lane = lax.broadcasted_iota(jnp.int32, shape, 1)
    return lane < LANE_BETA, (lane >= LANE_BETA) & (lane < LANE_G), (lane >= LANE_G) & (lane < LANE_G + NGH)


def _gates(proj, bias_vec, alog_vec):
    def body(s_ref, b_ref, a_ref, o_ref, carry_ref):
        i = pl.program_id(0)

        @pl.when(i == 0)
        def _():
            carry_ref[...] = jnp.zeros_like(carry_ref)

        z = s_ref[...] + b_ref[...]
        tail = jnp.log(1.0 + jnp.exp(-jnp.abs(z)))
        sp = jnp.maximum(z, 0.0) + tail
        lf = jnp.minimum(z, 0.0) - tail
        r = lax.broadcasted_iota(jnp.int32, (BQ, BQ), 0)
        c = lax.broadcasted_iota(jnp.int32, (BQ, BQ), 1)
        tri = (c <= r).astype(F32)
        cum = _hdot(tri, lf) + carry_ref[...]
        carry_ref[...] = cum[BQ - 1:BQ, :]
        is_fox, is_beta, is_g = _gate_lanes(z.shape)
        o_ref[...] = jnp.where(is_fox, cum, jnp.where(is_beta, _sigmoid(z), jnp.where(is_g, -jnp.exp(a_ref[...]) * sp, 0.0)))

    return pl.pallas_call(
        body, name="gates", grid=(NQ,),
        in_specs=[pl.BlockSpec((BQ, LANES), lambda i: (i, BLK_SMALL)), _vec_spec(LANES), _vec_spec(LANES)],
        out_specs=pl.BlockSpec((BQ, LANES), lambda i: (i, 0)), out_shape=jax.ShapeDtypeStruct((S, LANES), F32),
        scratch_shapes=[pltpu.VMEM((1, LANES), F32)], compiler_params=_cparams(),
    )(proj, bias_vec, alog_vec)


def _gates_bwd(proj, bias_vec, alog_vec, dgates_gdn, dcum_fox, dproj):
    def body(s_ref, b_ref, a_ref, dg_ref, dc_ref, dproj_in, dproj_ref, red_ref, carry_ref):
        del dproj_in
        i = pl.program_id(0)

        @pl.when(i == 0)
        def _():
            carry_ref[...] = jnp.zeros_like(carry_ref)
            red_ref[...] = jnp.zeros_like(red_ref)

        z = s_ref[...] + b_ref[...]
        dg = dg_ref[...] + dc_ref[...]
        r = lax.broadcasted_iota(jnp.int32, (BQ, BQ), 0)
        c = lax.broadcasted_iota(jnp.int32, (BQ, BQ), 1)
        upper = (c >= r).astype(F32)
        dlf = _hdot(upper, dg) + carry_ref[...]
        carry_ref[...] = dlf[0:1, :]
        sig = _sigmoid(z)
        g_scale = -jnp.exp(a_ref[...])
        is_fox, is_beta, is_g = _gate_lanes(z.shape)
        ds = jnp.where(is_fox, dlf * (1.0 - sig), jnp.where(is_beta, dg * sig * (1.0 - sig), jnp.where(is_g, dg * g_scale * sig, 0.0)))
        dproj_ref[:, 0:LANES] = ds.astype(BF16)
        dproj_ref[:, LANES:2 * LANES] = jnp.zeros((BQ, LANES), BF16)
        dalog = jnp.where(is_g, dg * g_scale * _softplus(z), 0.0)
        sums = jnp.sum(ds, axis=0, keepdims=True)
        red_ref[0:1, :] += jnp.where(is_fox[0:1], sums, 0.0)
        red_ref[1:2, :] += pltpu.roll(jnp.where(is_g[0:1], sums, 0.0), LANES - LANE_G, 1)
        red_ref[2:3, :] += pltpu.roll(jnp.sum(dalog, axis=0, keepdims=True), LANES - LANE_G, 1)

    blk = pl.BlockSpec((BQ, LANES), lambda i: (NQ - 1 - i, 0))
    return pl.pallas_call(
        body, name="gates_bwd", grid=(NQ,),
        in_specs=[pl.BlockSpec((BQ, LANES), lambda i: (NQ - 1 - i, BLK_SMALL)), _vec_spec(LANES), _vec_spec(LANES), blk, blk,
                  pl.BlockSpec(memory_space=pl.ANY)],
        out_specs=[pl.BlockSpec((BQ, 2 * LANES), lambda i: (NQ - 1 - i, BLK_SMALL // 2)), pl.BlockSpec((8, LANES), lambda i: (0, 0))],
        out_shape=[jax.ShapeDtypeStruct((S, DPROJ_PAD), BF16), jax.ShapeDtypeStruct((8, LANES), F32)],
        input_output_aliases={5: 0},
        scratch_shapes=[pltpu.VMEM((1, LANES), F32)], compiler_params=_cparams(),
    )(proj, bias_vec, alog_vec, dgates_gdn, dcum_fox, dproj)


FOX_SCALE = FHD ** -0.5
FOX_PAIRS = 2
FOX_PAIRS_BWD = 2


def _head_mask(e):
    lane = lax.broadcasted_iota(jnp.int32, (1, LANES), 1)
    return (lane >= e * FHD) & (lane < (e + 1) * FHD)


def _lane_col(vals, index):
    lane = lax.broadcasted_iota(jnp.int32, vals.shape, 1)
    return jnp.sum(jnp.where(lane == index, vals, 0.0), axis=1, keepdims=True)


def _sublane_row(vals, index):
    row = lax.broadcasted_iota(jnp.int32, vals.shape, 0)
    return jnp.sum(jnp.where(row == index, vals, 0.0), axis=0, keepdims=True)


def _pair_cols(c0, c1):
    lane = lax.broadcasted_iota(jnp.int32, (c0.shape[0], 2), 1)
    return jnp.where(lane == 0, c0, c1)


def _split3(x):
    hi = x.astype(BF16).astype(F32)
    rest = x - hi
    mid = rest.astype(BF16).astype(F32)
    return hi, mid, (rest - mid).astype(BF16).astype(F32)


def _fox_operand(vals, e, cum, is_query):
    lane = lax.broadcasted_iota(jnp.int32, (1, LANES), 1)
    base = (1 - e) * FHD
    parts = _split3(cum)
    own = jnp.where(_head_mask(e), vals * FOX_SCALE if is_query else vals, 0.0)
    cum_at, ones_at = (base, base + 3) if is_query else (base + 3, base)
    sign = 1.0 if is_query else -1.0
    out = own + jnp.where((lane >= ones_at) & (lane < ones_at + 3), 1.0, 0.0)
    for i, part in enumerate(parts):
        out = out + jnp.where(lane == cum_at + i, sign * part, 0.0)
    return out.astype(BF16)


def _causal_block():
    return lax.broadcasted_iota(jnp.int32, (BQ, BQ), 1) <= lax.broadcasted_iota(jnp.int32, (BQ, BQ), 0)


def _head_rms(o, masks):
    o2 = o * o
    r = [lax.rsqrt(jnp.sum(jnp.where(mk, o2, 0.0), axis=1, keepdims=True) * (1.0 / FHD) + EPS) for mk in masks]
    return jnp.where(masks[0], r[0], r[1])


def _hosted(exchange):
    if exchange is None:
        return [], [], [], [], []
    return (exchange.inputs, [HBM] * len(exchange.inputs), [HBM] * len(exchange.out_shape), exchange.out_shape,
            exchange.sem_shapes())


def _fox_fwd(proj, gates, w2, exchange=None):
    ex_in, ex_in_specs, ex_out_specs, ex_out_shape, ex_scratch = _hosted(exchange)

    n_in = 3 * FOX_PAIRS + 2
    heads = [(pp, e) for pp in range(FOX_PAIRS) for e in range(2)]

    def body(*refs):
        qkv_refs, g_ref, w_ref = refs[:3 * FOX_PAIRS], refs[3 * FOX_PAIRS], refs[3 * FOX_PAIRS + 1]
        mix_ref, o_ref, lse_ref = refs[n_in + len(ex_in):n_in + 3 + len(ex_in)]
        ka_ref, vb_ref = refs[n_in + 3 + len(ex_in) + len(ex_out_shape):n_in + 5 + len(ex_in) + len(ex_out_shape)]
        ex_refs = refs[n_in:n_in + len(ex_in)] + refs[n_in + 3 + len(ex_in):n_in + 3 + len(ex_in) + len(ex_out_shape)] + refs[-2:]
        grp, qi = pl.program_id(0), pl.program_id(1)

        def head_index(pp, e):
            return 2 * (FOX_PAIRS * grp + pp) + e

        if exchange is not None:
            @pl.when((grp == 0) & (qi == 0))
            def _():
                exchange.start(*exchange.split(ex_refs))

        @pl.when(qi == 0)
        def _():
            gt = g_ref[...]
            for pp in range(FOX_PAIRS):
                kv = qkv_refs[3 * pp + 1][...]
                for e in range(2):
                    ka_ref[2 * pp + e] = _fox_operand(kv, e, _lane_col(gt, head_index(pp, e)), False)
                vb_ref[pp] = qkv_refs[3 * pp + 2][...].astype(BF16)

        masks = [_head_mask(0), _head_mask(1)]
        gt = g_ref[pl.ds(pl.multiple_of(qi * BQ, BQ), BQ), :]
        qs = [_fox_operand(qkv_refs[3 * pp][...], e, _lane_col(gt, head_index(pp, e)), True) for pp, e in heads]
        n = range(len(heads))

        def block(kj, carry, diagonal):
            rows = pl.ds(pl.multiple_of(kj * BQ, BQ), BQ)
            s = [_dot(qs[i], ka_ref[i, rows, :], 1, 1) for i in n]
            if diagonal:
                s = [jnp.where(_causal_block(), s[i], -jnp.inf) for i in n]
            m_new = [jnp.maximum(carry[i][0], jnp.max(s[i], axis=-1, keepdims=True)) for i in n]
            p = [jnp.exp(s[i] - m_new[i]) for i in n]
            alpha = [jnp.exp(carry[i][0] - m_new[i]) for i in n]
            l_new = [alpha[i] * carry[i][1] + jnp.sum(p[i], axis=-1, keepdims=True) for i in n]
            pv = [_dot(p[i], vb_ref[heads[i][0], rows, :]) for i in n]
            return tuple((m_new[i], l_new[i], alpha[i] * carry[i][2] + pv[i]) for i in n)

        one = (jnp.full((BQ, 1), -jnp.inf, F32), jnp.zeros((BQ, 1), F32), jnp.zeros((BQ, LANES), F32))
        below = lax.fori_loop(0, qi, lambda kj, carry: block(kj, carry, False), (one,) * len(heads))
        done = block(qi, below, True)
        for pp in range(FOX_PAIRS):
            (m0, l0, a0), (m1, l1, a1) = done[2 * pp], done[2 * pp + 1]
            o = jnp.where(masks[0], a0 / l0, a1 / l1)
            cols = slice(pp * LANES, (pp + 1) * LANES)
            o_ref[:, cols] = o
            mix_ref[:, cols] = (o * _head_rms(o, masks) * w_ref[...]).astype(BF16)
            lse_ref[pp] = _pair_cols(m0 + jnp.log(l0), m1 + jnp.log(l1))

        if exchange is not None:
            @pl.when((grp == NPAIR // FOX_PAIRS // 2) & (qi == 0))
            def _():
                exchange.middle(*exchange.split(ex_refs))

            @pl.when((grp == NPAIR // FOX_PAIRS - 1) & (qi == NQ - 1))
            def _():
                exchange.rest(*exchange.split(ex_refs))

    qkv_specs = []
    for pp in range(FOX_PAIRS):
        qkv_specs.append(pl.BlockSpec((BQ, LANES), lambda g, i, pp=pp: (i, 3 * (FOX_PAIRS * g + pp))))
        qkv_specs.append(pl.BlockSpec((S, LANES), lambda g, i, pp=pp: (0, 3 * (FOX_PAIRS * g + pp) + 1)))
        qkv_specs.append(pl.BlockSpec((S, LANES), lambda g, i, pp=pp: (0, 3 * (FOX_PAIRS * g + pp) + 2)))
    blk = pl.BlockSpec((BQ, FOX_PAIRS * LANES), lambda g, i: (i, g))
    res = pl.pallas_call(
        body, name="fox_fwd", grid=(NPAIR // FOX_PAIRS, NQ),
        in_specs=qkv_specs + [pl.BlockSpec((S, LANES), lambda g, i: (0, 0)), pl.BlockSpec((1, LANES), lambda g, i: (0, 0))]
        + ex_in_specs,
        out_specs=[blk, blk, pl.BlockSpec((FOX_PAIRS, BQ, 2), lambda g, i: (g, i, 0))] + ex_out_specs,
        out_shape=[jax.ShapeDtypeStruct((S, D), BF16), jax.ShapeDtypeStruct((S, DFOX), F32),
                   jax.ShapeDtypeStruct((NPAIR, S, 2), F32)] + ex_out_shape,
        scratch_shapes=[pltpu.VMEM((2 * FOX_PAIRS, S, LANES), BF16), pltpu.VMEM((FOX_PAIRS, S, LANES), BF16)] + ex_scratch,
        compiler_params=_cparams(),
    )(*([proj] * (3 * FOX_PAIRS)), gates, w2, *ex_in)
    return res[0], res[1], res[2], res[3:]


def _fox_norm_bwd(o, dmix, w2, exchange=None):
    ex_in, ex_in_specs, ex_out_specs, ex_out_shape, ex_scratch = _hosted(exchange)

    def body(*refs):
        o_ref, g_ref, w_ref = refs[:3]
        do_ref, dl_ref, dw_ref = refs[3 + len(ex_in):6 + len(ex_in)]
        ex_refs = refs[3:3 + len(ex_in)] + refs[6 + len(ex_in):]
        hp, qi = pl.program_id(0), pl.program_id(1)

        if exchange is not None:
            @pl.when((hp == 0) & (qi == 0))
            def _():
                exchange.start(*exchange.split(ex_refs))

        masks = [_head_mask(0), _head_mask(1)]
        ov = o_ref[...]
        g = g_ref[...]
        r = _head_rms(ov, masks)
        gw = g * w_ref[...]
        gwo = gw * ov
        mean = [jnp.sum(jnp.where(mk, gwo, 0.0), axis=1, keepdims=True) * (1.0 / FHD) for mk in masks]
        do = r * gw - ov * (r * r * r) * jnp.where(masks[0], mean[0], mean[1])
        do_ref[...] = do.astype(BF16)
        doo = do * ov
        dl_ref[...] = _pair_cols(*[jnp.sum(jnp.where(mk, doo, 0.0), axis=1, keepdims=True) for mk in masks])

        @pl.when((hp == 0) & (qi == 0))
        def _():
            dw_ref[...] = jnp.zeros_like(dw_ref)

        dw_ref[...] += jnp.sum(g * ov * r, axis=0, keepdims=True)

        @pl.when((hp == NPAIR - 1) & (qi == NQ - 1))
        def _():
            dw = dw_ref[...]
            dw_ref[...] = dw + pltpu.roll(dw, FHD, 1)
            if exchange is not None:
                exchange.finish(*exchange.split(ex_refs))

    blk = pl.BlockSpec((BQ, LANES), lambda hp, i: (i, hp))
    vec = pl.BlockSpec((1, LANES), lambda hp, i: (0, 0))
    res = pl.pallas_call(
        body, name="fox_norm_bwd", grid=(NPAIR, NQ), in_specs=[blk, blk, vec] + ex_in_specs,
        out_specs=[blk, pl.BlockSpec((None, BQ, 2), lambda hp, i: (hp, i, 0)), vec] + ex_out_specs,
        out_shape=[jax.ShapeDtypeStruct((S, DFOX), BF16), jax.ShapeDtypeStruct((NPAIR, S, 2), F32),
                   jax.ShapeDtypeStruct((1, LANES), F32)] + ex_out_shape,
        scratch_shapes=ex_scratch, compiler_params=_cparams(),
    )(o, dmix, w2, *ex_in)
    return res[0], res[1], res[2], res[3:]


def _fox_bwd(proj, do, gates, lse, delta, exchange=None):
    ex_in, ex_in_specs, ex_out_specs, ex_out_shape, ex_scratch = _hosted(exchange)

    pg = FOX_PAIRS_BWD
    n_in = 3 * pg + 4
    heads = [(pp, e) for pp in range(pg) for e in range(2)]

    def body(*refs):
        qkv_refs = refs[:3 * pg]
        do_ref, g_ref, lse_ref, dl_ref = refs[3 * pg:n_in]
        dproj_ref, dc_ref = refs[n_in + len(ex_in):n_in + 2 + len(ex_in)]
        qa_ref, dq_ref = refs[n_in + 2 + len(ex_in) + len(ex_out_shape):n_in + 4 + len(ex_in) + len(ex_out_shape)]
        ex_refs = refs[n_in:n_in + len(ex_in)] + refs[n_in + 2 + len(ex_in):n_in + 2 + len(ex_in) + len(ex_out_shape)] + refs[-2:]
        grp, kj = pl.program_id(0), pl.program_id(1)

        def head_index(pp, e):
            return 2 * (pg * grp + pp) + e

        if exchange is not None:
            @pl.when((grp == 0) & (kj == 0))
            def _():
                exchange.start(*exchange.split(ex_refs))

        @pl.when(kj == 0)
        def _():
            gt = g_ref[...]
            for pp in range(pg):
                qv = qkv_refs[3 * pp][...]
                for e in range(2):
                    qa_ref[2 * pp + e] = _fox_operand(qv, e, _lane_col(gt, head_index(pp, e)), True)
            dq_ref[...] = jnp.zeros_like(dq_ref)

        @pl.when((grp == 0) & (kj == 0))
        def _():
            dc_ref[...] = jnp.zeros_like(dc_ref)

        masks = [_head_mask(0), _head_mask(1)]
        krows = pl.ds(pl.multiple_of(kj * BQ, BQ), BQ)
        gk = g_ref[krows, :]
        kas = [_fox_operand(qkv_refs[3 * pp + 1][...], e, _lane_col(gk, head_index(pp, e)), False) for pp, e in heads]
        vbs = [qkv_refs[3 * pp + 2][...].astype(BF16) for pp in range(pg)]
        lane = lax.broadcasted_iota(jnp.int32, (BQ, LANES), 1)
        n = range(len(heads))

        def block(qi, carry, diagonal):
            dks, dvs, css = carry
            rows = pl.ds(pl.multiple_of(qi * BQ, BQ), BQ)
            qa = [qa_ref[i, rows, :] for i in n]
            s = [_dot(qa[i], kas[i], 1, 1) for i in n]
            if diagonal:
                s = [jnp.where(_causal_block(), s[i], -jnp.inf) for i in n]
            dov = [do_ref[rows, pp * LANES:(pp + 1) * LANES] for pp in range(pg)]
            doe = [jnp.where(masks[e], dov[pp], jnp.zeros_like(dov[pp])) for pp, e in heads]
            lse2 = [lse_ref[pp, rows, :] for pp in range(pg)]
            dl2 = [dl_ref[pp, rows, :] for pp in range(pg)]
            p = [jnp.exp(s[i] - _lane_col(lse2[heads[i][0]], heads[i][1])) for i in n]
            dp = [_dot(doe[i], vbs[heads[i][0]], 1, 1) for i in n]
            ds = [p[i] * (dp[i] - _lane_col(dl2[heads[i][0]], heads[i][1])) for i in n]
            dv_part = [_dot(p[i], doe[i], 0, 0) for i in n]
            dk_part = [_dot(ds[i], jnp.where(masks[heads[i][1]], qa[i], jnp.zeros_like(qa[i])), 0, 0) for i in n]
            dq_part = [jnp.where(masks[heads[i][1]], _dot(ds[i], kas[i]), 0.0) for i in n]
            css = tuple(css[i] + jnp.sum(ds[i], axis=0, keepdims=True) for i in n)
            dc = jnp.zeros((BQ, LANES), F32)
            for i in n:
                dc = dc + jnp.where(lane == head_index(*heads[i]), jnp.sum(ds[i], axis=1, keepdims=True), 0.0)
            for pp in range(pg):
                dq_ref[pp, rows, :] += (dq_part[2 * pp] + dq_part[2 * pp + 1]) * FOX_SCALE
            dc_ref[rows, :] += dc
            dks = tuple(dks[pp] + dk_part[2 * pp] + dk_part[2 * pp + 1] for pp in range(pg))
            dvs = tuple(dvs[pp] + dv_part[2 * pp] + dv_part[2 * pp + 1] for pp in range(pg))
            return dks, dvs, css

        zero = jnp.zeros((BQ, LANES), F32)
        first = block(kj, ((zero,) * pg, (zero,) * pg, (jnp.zeros((1, BQ), F32),) * len(heads)), True)
        dks, dvs, css = lax.fori_loop(kj + 1, NQ, lambda qi, carry: block(qi, carry, False), first)
        r = lax.broadcasted_iota(jnp.int32, (BQ, BQ), 0)
        c = lax.broadcasted_iota(jnp.int32, (BQ, BQ), 1)
        dcol = jnp.zeros((BQ, LANES), F32)
        for i in n:
            col = jnp.sum(jnp.where(r == c, css[i], 0.0), axis=1, keepdims=True)
            dcol = dcol + jnp.where(lane == head_index(*heads[i]), col, 0.0)
        dc_ref[krows, :] -= dcol
        for pp in range(pg):
            base = 3 * pp * LANES
            dproj_ref[krows, base + LANES:base + 2 * LANES] = dks[pp].astype(BF16)
            dproj_ref[krows, base + 2 * LANES:base + 3 * LANES] = dvs[pp].astype(BF16)

        @pl.when(kj == NQ - 1)
        def _():
            for pp in range(pg):
                dproj_ref[:, 3 * pp * LANES:(3 * pp + 1) * LANES] = dq_ref[pp].astype(BF16)

        if exchange is not None:
            @pl.when((grp == NPAIR // pg // 2) & (kj == 0))
            def _():
                exchange.middle(*exchange.split(ex_refs))

            @pl.when((grp == NPAIR // pg - 1) & (kj == NQ - 1))
            def _():
                exchange.rest(*exchange.split(ex_refs))

    qkv_specs = []
    for pp in range(pg):
        qkv_specs.append(pl.BlockSpec((S, LANES), lambda g, j, pp=pp: (0, 3 * (pg * g + pp))))
        qkv_specs.append(pl.BlockSpec((BQ, LANES), lambda g, j, pp=pp: (j, 3 * (pg * g + pp) + 1)))
        qkv_specs.append(pl.BlockSpec((BQ, LANES), lambda g, j, pp=pp: (j, 3 * (pg * g + pp) + 2)))
    pair = pl.BlockSpec((pg, S, 2), lambda g, j: (g, 0, 0))
    res = pl.pallas_call(
        body, name="fox_bwd", grid=(NPAIR // pg, NQ),
        in_specs=qkv_specs + [pl.BlockSpec((S, pg * LANES), lambda g, j: (0, g)), pl.BlockSpec((S, LANES), lambda g, j: (0, 0)),
                              pair, pair] + ex_in_specs,
        out_specs=[pl.BlockSpec((S, 3 * pg * LANES), lambda g, j: (0, g)), pl.BlockSpec((S, LANES), lambda g, j: (0, 0))]
        + ex_out_specs,
        out_shape=[jax.ShapeDtypeStruct((S, DPROJ_PAD), BF16), jax.ShapeDtypeStruct((S, LANES), F32)] + ex_out_shape,
        scratch_shapes=[pltpu.VMEM((2 * pg, S, LANES), BF16), pltpu.VMEM((pg, S, LANES), F32)] + ex_scratch,
        compiler_params=_cparams(),
    )(*([proj] * (3 * pg)), do, gates, lse, delta, *ex_in)
    return res[0], res[1], res[2:]


NQKV = 3 * NGH
GDN_QSCALE = GHD ** -0.5


def _shift_down(x, s):
    if s == 0:
        return x
    row = lax.broadcasted_iota(jnp.int32, x.shape, 0)
    return jnp.where(row >= s, pltpu.roll(x, s, 0), 0.0)


def _shift_up(x, s):
    if s == 0:
        return x
    n = x.shape[0]
    row = lax.broadcasted_iota(jnp.int32, x.shape, 0)
    return jnp.where(row < n - s, pltpu.roll(x, n - s, 0), 0.0)


def _conv_taps(xv):
    return [_shift_down(xv, CONV_K - 1 - j) for j in range(CONV_K)]


def _conv_pre(taps, wv):
    pre = taps[CONV_K - 1] * wv[CONV_K - 1:CONV_K, :]
    for j in range(CONV_K - 1):
        pre = pre + taps[j] * wv[j:j + 1, :]
    return pre


def _l2_factors(b):
    return b < 2 * NGH, jnp.where(b < NGH, GDN_QSCALE, 1.0)


def _gdn_pre(proj, conv_w):
    def body(x_ref, w_ref, o_ref):
        b = pl.program_id(0)
        c = _silu(_conv_pre(_conv_taps(x_ref[...]), w_ref[...]))
        normed, scale = _l2_factors(b)
        rs = lax.rsqrt(jnp.sum(c * c, axis=-1, keepdims=True) + EPS)
        o_ref[...] = c * jnp.where(normed, rs, 1.0) * scale

    return pl.pallas_call(
        body, name="gdn_pre", grid=(NQKV,),
        in_specs=[pl.BlockSpec((S, GHD), lambda b: (0, BLK_GDN + b)), pl.BlockSpec((CONV_K, GHD), lambda b: (0, b))],
        out_specs=pl.BlockSpec((S, GHD), lambda b: (0, b)),
        out_shape=jax.ShapeDtypeStruct((S, NQKV * GHD), F32), compiler_params=_cparams(),
    )(proj, conv_w)


def _gdn_pre_bwd(proj, conv_w, dqkv, dproj):
    def body(x_ref, w_ref, dy_ref, dproj_in, dx_ref, dw_ref):
        del dproj_in
        b = pl.program_id(0)
        taps = _conv_taps(x_ref[...])
        wv = w_ref[...]
        pre = _conv_pre(taps, wv)
        sig = _sigmoid(pre)
        c = pre * sig
        normed, scale = _l2_factors(b)
        g = dy_ref[...] * scale
        rs = lax.rsqrt(jnp.sum(c * c, axis=-1, keepdims=True) + EPS)
        dc_n = rs * g - c * (rs * rs * rs) * jnp.sum(g * c, axis=-1, keepdims=True)
        dc = jnp.where(normed, dc_n, g)
        dpre = dc * sig * (1.0 + pre * (1.0 - sig))
        dx = dpre * wv[CONV_K - 1:CONV_K, :]
        for j in range(CONV_K - 1):
            dx = dx + _shift_up(dpre, CONV_K - 1 - j) * wv[j:j + 1, :]
        dx_ref[...] = dx.astype(BF16)
        for j in range(CONV_K):
            dw_ref[j:j + 1, :] = jnp.sum(dpre * taps[j], axis=0, keepdims=True)

    return pl.pallas_call(
        body, name="gdn_pre_bwd", grid=(NQKV,),
        in_specs=[pl.BlockSpec((S, GHD), lambda b: (0, BLK_GDN + b)), pl.BlockSpec((CONV_K, GHD), lambda b: (0, b)),
                  pl.BlockSpec((None, S, GHD), lambda b: (b // NGH, 0, b % NGH)), pl.BlockSpec(memory_space=pl.ANY)],
        out_specs=[pl.BlockSpec((S, GHD), lambda b: (0, BLK_GDN + b)), pl.BlockSpec((CONV_K, GHD), lambda b: (0, b))],
        out_shape=[jax.ShapeDtypeStruct((S, DPROJ_PAD), BF16), jax.ShapeDtypeStruct((CONV_K, NQKV * GHD), F32)],
        input_output_aliases={3: 0}, compiler_params=_cparams(),
    )(proj, conv_w, dqkv, dproj)


CB = 16
NCB = NCH // CB


def _chunk_prep(qs, ks, vs, gcols, bcols, t_saved=None):
    n = range(len(qs))
    r = lax.broadcasted_iota(jnp.int32, (CHUNK, CHUNK), 0)
    c = lax.broadcasted_iota(jnp.int32, (CHUNK, CHUNK), 1)
    incl = c <= r
    eye = (r == c).astype(F32)
    grow = [jnp.sum(gcols[i] * eye, axis=0, keepdims=True) for i in n]
    gc_col = [jnp.sum(jnp.where(incl, grow[i], 0.0), axis=1, keepdims=True) for i in n]
    gc_row = [jnp.sum(jnp.where(r <= c, gcols[i], 0.0), axis=0, keepdims=True) for i in n]
    decay = [jnp.exp(jnp.where(incl, gc_col[i] - gc_row[i], -jnp.inf)) for i in n]
    kb = [ks[i] * bcols[i] for i in n]
    vb = [vs[i] * bcols[i] for i in n]
    kk = [_mm_nt(kb[i], ks[i]) for i in n]
    m = [jnp.where(c < r, kk[i] * decay[i], 0.0) for i in n]
    if t_saved is None:
        t_inv = [eye - m[i] for i in n]
        p = [_dot3(m[i], m[i]) for i in n]
        for step in range(5):
            t_inv = [t_inv[i] + _dot3(t_inv[i], p[i]) for i in n]
            if step < 4:
                p = [_dot3(p[i], p[i]) for i in n]
    else:
        t_inv = [_saved_inverse(m[i], t_saved[i]) for i in n]
    egc = [jnp.exp(gc_col[i]) for i in n]
    u = [_mm_nn(t_inv[i], vb[i]) for i in n]
    w = [_mm_nn(t_inv[i], kb[i] * egc[i]) for i in n]
    qk = [_mm_nt(qs[i], ks[i]) for i in n]
    gc_last = [gc_col[i][CHUNK - 1:CHUNK, :] for i in n]
    return [(u[i], w[i], qk[i] * decay[i], qs[i] * egc[i], ks[i] * jnp.exp(gc_last[i] - gc_col[i]), jnp.exp(gc_last[i]),
             t_inv[i]) for i in n]


def _prep_specs():
    rows = CB * CHUNK
    qs = pl.BlockSpec((rows, GHD), lambda i, h: (i, h))
    ks = pl.BlockSpec((rows, GHD), lambda i, h: (i, NGH + h))
    vs = pl.BlockSpec((rows, GHD), lambda i, h: (i, 2 * NGH + h))
    gs = pl.BlockSpec((rows, LANES), lambda i, h: (i, 0))
    a_s = pl.BlockSpec((None, rows, CHUNK), lambda i, h: (h, i, 0))
    gl_s = pl.BlockSpec((None, CB, 1, LANES), lambda i, h: (h, i, 0, 0))
    return qs, ks, vs, gs, a_s, gl_s


def _gdn_prep(qkv, gates, exchange=None):
    ex_in, ex_in_specs, ex_out_specs, ex_out_shape, ex_scratch = _hosted(exchange)

    def body(*refs):
        q_ref, k_ref, v_ref, g_ref = refs[:4]
        u_ref, w_ref, qd_ref, kd_ref, a_ref, gl_ref, t_ref = refs[4 + len(ex_in):11 + len(ex_in)]
        ex_refs = refs[4:4 + len(ex_in)] + refs[11 + len(ex_in):]
        h = pl.program_id(1)

        if exchange is not None:
            @pl.when((pl.program_id(0) == 0) & (h == 0))
            def _():
                exchange.start(*exchange.split(ex_refs))

        chunks = [pl.ds(cidx * CHUNK, CHUNK) for cidx in range(CB)]
        gts = [g_ref[rows, :] for rows in chunks]
        outs = _chunk_prep([q_ref[rows, :] for rows in chunks], [k_ref[rows, :] for rows in chunks],
                           [v_ref[rows, :] for rows in chunks], [_lane_col(gt, LANE_G + h) for gt in gts],
                           [_lane_col(gt, LANE_BETA + h) for gt in gts])
        for cidx, rows in enumerate(chunks):
            u, w, a, qd, kd, gl, t_inv = outs[cidx]
            u_ref[rows, :] = u
            w_ref[rows, :] = w
            qd_ref[rows, :] = qd
            kd_ref[rows, :] = kd
            a_ref[rows, :] = a
            t_ref[rows, :] = t_inv
            gl_ref[cidx] = jnp.broadcast_to(gl, (1, LANES))

        if exchange is not None:
            @pl.when((pl.program_id(0) == NCB // 2) & (h == 0))
            def _():
                exchange.middle(*exchange.split(ex_refs))

            @pl.when((pl.program_id(0) == NCB - 1) & (h == NGH - 1))
            def _():
                exchange.rest(*exchange.split(ex_refs))

    qs, ks, vs, gs, a_s, gl_s = _prep_specs()
    tok = jax.ShapeDtypeStruct((S, DGDN), F32)
    sq = jax.ShapeDtypeStruct((NGH, S, CHUNK), F32)
    res = pl.pallas_call(
        body, name="gdn_prep", grid=(NCB, NGH), in_specs=[qs, ks, vs, gs] + ex_in_specs,
        out_specs=[qs, qs, qs, qs, a_s, gl_s, a_s] + ex_out_specs,
        out_shape=[tok, tok, tok, tok, sq, jax.ShapeDtypeStruct((NGH, NCH, 1, LANES), F32), sq] + ex_out_shape,
        scratch_shapes=ex_scratch, compiler_params=_cparams(),
    )(qkv, qkv, qkv, gates, *ex_in)
    return res[:7], res[7:]


def _gdn_prep_bwd(qkv, gates, t_inv, du, dw, dqd, dkd, da, dgl, exchange=None):
    ex_in, ex_in_specs, ex_out_specs, ex_out_shape, ex_scratch = _hosted(exchange)

    def body(*refs):
        q_ref, k_ref, v_ref, g_ref, t_ref, du_ref, dw_ref, dqd_ref, dkd_ref, da_ref, dgl_ref = refs[:11]
        dqkv_ref, dg_ref = refs[11 + len(ex_in):13 + len(ex_in)]
        ex_refs = refs[11:11 + len(ex_in)] + refs[13 + len(ex_in):]
        h = pl.program_id(1)

        if exchange is not None:
            @pl.when((pl.program_id(0) == 0) & (h == 0))
            def _():
                exchange.start(*exchange.split(ex_refs))

        @pl.when(h == 0)
        def _():
            dg_ref[...] = jnp.zeros_like(dg_ref)

        lane = lax.broadcasted_iota(jnp.int32, (CHUNK, LANES), 1)
        chunks = [pl.ds(cidx * CHUNK, CHUNK) for cidx in range(CB)]
        gts = [g_ref[rows, :] for rows in chunks]
        t_saved = [t_ref[rows, :] for rows in chunks]
        _, vjp = jax.vjp(lambda *args: [o[:6] for o in _chunk_prep(*args, t_saved=t_saved)],
                         [q_ref[rows, :] for rows in chunks], [k_ref[rows, :] for rows in chunks],
                         [v_ref[rows, :] for rows in chunks], [_lane_col(gt, LANE_G + h) for gt in gts],
                         [_lane_col(gt, LANE_BETA + h) for gt in gts])
        dqs, dks, dvs, dgcs, dbcs = vjp([(du_ref[rows, :], dw_ref[rows, :], da_ref[rows, :], dqd_ref[rows, :],
                                          dkd_ref[rows, :], dgl_ref[cidx][:, 0:1]) for cidx, rows in enumerate(chunks)])
        for cidx, rows in enumerate(chunks):
            dq, dk, dv, dgc, dbc = dqs[cidx], dks[cidx], dvs[cidx], dgcs[cidx], dbcs[cidx]
            dqkv_ref[0, rows, :] = dq
            dqkv_ref[1, rows, :] = dk
            dqkv_ref[2, rows, :] = dv
            dg_ref[rows, :] += jnp.where(lane == LANE_G + h, dgc, 0.0) + jnp.where(lane == LANE_BETA + h, dbc, 0.0)

        if exchange is not None:
            @pl.when((pl.program_id(0) == NCB - 1) & (h == NGH - 1))
            def _():
                exchange.finish(*exchange.split(ex_refs))

    qs, ks, vs, gs, a_s, gl_s = _prep_specs()
    res = pl.pallas_call(
        body, name="gdn_prep_bwd", grid=(NCB, NGH), in_specs=[qs, ks, vs, gs, a_s, qs, qs, qs, qs, a_s, gl_s] + ex_in_specs,
        out_specs=[pl.BlockSpec((3, CB * CHUNK, GHD), lambda i, h: (0, i, h)), gs] + ex_out_specs,
        out_shape=[jax.ShapeDtypeStruct((3, S, DGDN), F32), jax.ShapeDtypeStruct((S, LANES), F32)] + ex_out_shape,
        scratch_shapes=ex_scratch, compiler_params=_cparams(),
    )(qkv, qkv, qkv, gates, t_inv, du, dw, dqd, dkd, da, dgl, *ex_in)
    return res[0], res[1], res[2:]


def _scan_specs(nh, parts, reverse):
    wide, rows, chunks = nh * GHD, S // parts, NCH // parts

    def part(p):
        return parts - 1 - p if reverse else p

    hs = pl.BlockSpec((rows, wide), lambda g, p: (part(p), g))
    a_s = pl.BlockSpec((nh, rows, CHUNK), lambda g, p: (g, part(p), 0))
    gl_s = pl.BlockSpec((nh, chunks, 1, LANES), lambda g, p: (g, part(p), 0, 0))
    st_s = pl.BlockSpec((nh, chunks, GHD, GHD), lambda g, p: (g, part(p), 0, 0))
    gz_s = pl.BlockSpec((rows, wide), lambda g, p: (part(p), BLK_GZ // nh + g))
    mix_s = pl.BlockSpec((rows, wide), lambda g, p: (part(p), NPAIR // nh + g))
    return hs, a_s, gl_s, st_s, gz_s, mix_s


def _head_cols(hh):
    return slice(hh * GHD, (hh + 1) * GHD)


SCAN_HEADS, SCAN_PARTS = 4, 2
SCAN_HEADS_BWD, SCAN_PARTS_BWD = 4, 4


def _gdn_scan(u, w, qd, kd, a, gl, proj, w_norm, mix):
    heads = range(SCAN_HEADS)

    def body(u_ref, w_ref, qd_ref, kd_ref, a_ref, gl_ref, z_ref, wn_ref, mix_in, mix_ref, o_ref, st_ref, carry_ref):
        del mix_in

        @pl.when(pl.program_id(1) == 0)
        def _():
            carry_ref[...] = jnp.zeros_like(carry_ref)

        def step(ci, states):
            rows = pl.ds(pl.multiple_of(ci * CHUNK, CHUNK), CHUNK)
            for hh in heads:
                st_ref[hh, ci] = states[hh]
            ws = [_dot(w_ref[rows, _head_cols(hh)], states[hh]) for hh in heads]
            qs = [_dot(qd_ref[rows, _head_cols(hh)], states[hh]) for hh in heads]
            vn = [u_ref[rows, _head_cols(hh)] - ws[hh] for hh in heads]
            av = [_dot(a_ref[hh, rows, :], vn[hh]) for hh in heads]
            kv = [_dot(kd_ref[rows, _head_cols(hh)], vn[hh], 0, 0) for hh in heads]
            for hh in heads:
                o_ref[rows, _head_cols(hh)] = qs[hh] + av[hh]
            return tuple(states[hh] * gl_ref[hh, ci] + kv[hh] for hh in heads)

        last = lax.fori_loop(0, NCH // SCAN_PARTS, step, tuple(carry_ref[hh] for hh in heads))
        for hh in heads:
            carry_ref[hh] = last[hh]
            ov = o_ref[:, _head_cols(hh)]
            mix_ref[:, _head_cols(hh)] = (ov * _rms_scale(ov) * wn_ref[...] * _silu(z_ref[:, _head_cols(hh)])).astype(BF16)

    hs, a_s, gl_s, st_s, gz_s, mix_s = _scan_specs(SCAN_HEADS, SCAN_PARTS, False)
    return pl.pallas_call(
        body, name="gdn_scan", grid=(NGH // SCAN_HEADS, SCAN_PARTS),
        in_specs=[hs, hs, hs, hs, a_s, gl_s, gz_s, pl.BlockSpec((1, GHD), lambda g, p: (0, 0)),
                  pl.BlockSpec(memory_space=pl.ANY)],
        out_specs=[mix_s, hs, st_s],
        out_shape=[jax.ShapeDtypeStruct((S, D), BF16), jax.ShapeDtypeStruct((S, DGDN), F32),
                   jax.ShapeDtypeStruct((NGH, NCH, GHD, GHD), F32)],
        input_output_aliases={8: 0}, scratch_shapes=[pltpu.VMEM((SCAN_HEADS, GHD, GHD), F32)], compiler_params=_cparams(),
    )(u, w, qd, kd, a, gl, proj, w_norm, mix)


def _gdn_scan_bwd(dmix, o, proj, w_norm, u, w, qd, kd, a, gl, states, dproj, exchange=None):
    ex_in, ex_in_specs, ex_out_specs, ex_out_shape, ex_scratch = _hosted(exchange)
    groups = NGH // SCAN_HEADS_BWD

    def body(*refs):
        dy_ref, o_ref, z_ref, wn_ref, u_ref, w_ref, qd_ref, kd_ref, a_ref, gl_ref, st_ref = refs[:11]
        dz_ref, du_ref, dw_ref, dqd_ref, dkd_ref, da_ref, dgl_ref, dwn_ref = refs[12 + len(ex_in):20 + len(ex_in)]
        do_ref, carry_ref = refs[20 + len(ex_in) + len(ex_out_shape):22 + len(ex_in) + len(ex_out_shape)]
        ex_refs = refs[12:12 + len(ex_in)] + refs[20 + len(ex_in):20 + len(ex_in) + len(ex_out_shape)] + refs[-2:]
        heads = range(SCAN_HEADS_BWD)
        chunks = NCH // SCAN_PARTS_BWD

        if exchange is not None:
            @pl.when((pl.program_id(0) == 0) & (pl.program_id(1) == 0))
            def _():
                exchange.start(*exchange.split(ex_refs))

        @pl.when((pl.program_id(0) == 0) & (pl.program_id(1) == 0))
        def _():
            dwn_ref[...] = jnp.zeros_like(dwn_ref)

        @pl.when(pl.program_id(1) == 0)
        def _():
            carry_ref[...] = jnp.zeros_like(carry_ref)

        wn = wn_ref[...]
        for hh in heads:
            c = _head_cols(hh)
            ov = o_ref[:, c]
            zv = z_ref[:, c]
            g = dy_ref[:, c]
            sig = _sigmoid(zv)
            dz_ref[:, c] = (g * (ov * _rms_scale(ov) * wn) * sig * (1.0 + zv * (1.0 - sig))).astype(BF16)
            do, dwt = _rms_bwd(ov, wn, g * zv * sig)
            do_ref[:, c] = do
            dwn_ref[...] += jnp.sum(dwt, axis=0, keepdims=True)

        def step(t, dstates):
            ci = chunks - 1 - t
            rows = pl.ds(pl.multiple_of(ci * CHUNK, CHUNK), CHUNK)
            cols = [_head_cols(hh) for hh in heads]
            state = [st_ref[hh, ci] for hh in heads]
            dov = [do_ref[rows, cols[hh]] for hh in heads]
            wv = [w_ref[rows, cols[hh]] for hh in heads]
            ws = [_dot(wv[hh], state[hh]) for hh in heads]
            adov = [_dot(a_ref[hh, rows, :], dov[hh], 0, 0) for hh in heads]
            kds = [_dot(kd_ref[rows, cols[hh]], dstates[hh]) for hh in heads]
            dqd = [_dot(dov[hh], state[hh], 1, 1) for hh in heads]
            qdo = [_dot(qd_ref[rows, cols[hh]], dov[hh], 0, 0) for hh in heads]
            vn = [u_ref[rows, cols[hh]] - ws[hh] for hh in heads]
            dvn = [adov[hh] + kds[hh] for hh in heads]
            da = [_dot(dov[hh], vn[hh], 1, 1) for hh in heads]
            dkd = [_dot(vn[hh], dstates[hh], 1, 1) for hh in heads]
            dwv = [_dot(dvn[hh], state[hh], 1, 1) for hh in heads]
            wdv = [_dot(wv[hh], dvn[hh], 0, 0) for hh in heads]
            for hh in heads:
                da_ref[hh, rows, :] = da[hh]
                dqd_ref[rows, cols[hh]] = dqd[hh]
                dkd_ref[rows, cols[hh]] = dkd[hh]
                dgl = jnp.sum(jnp.sum(dstates[hh] * state[hh], axis=1, keepdims=True), axis=0, keepdims=True)
                dgl_ref[hh, ci] = jnp.broadcast_to(dgl, (1, LANES))
                du_ref[rows, cols[hh]] = dvn[hh]
                dw_ref[rows, cols[hh]] = -dwv[hh]
            return tuple(dstates[hh] * gl_ref[hh, ci] + qdo[hh] - wdv[hh] for hh in heads)

        last = lax.fori_loop(0, chunks, step, tuple(carry_ref[hh] for hh in heads))
        for hh in heads:
            carry_ref[hh] = last[hh]

        if exchange is not None:
            @pl.when((pl.program_id(0) == groups - 1) & (pl.program_id(1) == SCAN_PARTS_BWD - 1))
            def _():
                exchange.finish(*exchange.split(ex_refs))

    hs, a_s, gl_s, st_s, gz_s, mix_s = _scan_specs(SCAN_HEADS_BWD, SCAN_PARTS_BWD, True)
    vec = pl.BlockSpec((1, GHD), lambda g, p: (0, 0))
    tok = jax.ShapeDtypeStruct((S, DGDN), F32)
    res = pl.pallas_call(
        body, name="gdn_scan_bwd", grid=(groups, SCAN_PARTS_BWD),
        in_specs=[mix_s, hs, gz_s, vec, hs, hs, hs, hs, a_s, gl_s, st_s, pl.BlockSpec(memory_space=pl.ANY)] + ex_in_specs,
        out_specs=[gz_s, hs, hs, hs, hs, a_s, gl_s, vec] + ex_out_specs,
        out_shape=[jax.ShapeDtypeStruct((S, DPROJ_PAD), BF16), tok, tok, tok, tok,
                   jax.ShapeDtypeStruct((NGH, S, CHUNK), F32), jax.ShapeDtypeStruct((NGH, NCH, 1, LANES), F32),
                   jax.ShapeDtypeStruct((1, GHD), F32)] + ex_out_shape,
        input_output_aliases={11: 0},
        scratch_shapes=[pltpu.VMEM((S // SCAN_PARTS_BWD, SCAN_HEADS_BWD * GHD), F32),
                        pltpu.VMEM((SCAN_HEADS_BWD, GHD, GHD), F32)] + ex_scratch,
        compiler_params=_cparams(),
    )(dmix, o, proj, w_norm, u, w, qd, kd, a, gl, states, dproj, *ex_in)
    return res[:8], res[8:]


def _place():
    return lax.axis_index("x"), lax.axis_index("y"), lax.axis_index("c")


def _place_scalars():
    x, y, c = _place()
    return jnp.stack([2 * x + y, c]).astype(jnp.int32)


def _other_chips(x, y):
    return [(1 - x, y), (x, 1 - y), (1 - x, 1 - y)]


HBM = pl.BlockSpec(memory_space=pltpu.HBM)
VMEM = pl.BlockSpec(memory_space=pltpu.VMEM)


def _half_rows(ref_or_rows, half):
    rows = ref_or_rows // 2
    return pl.ds(pl.multiple_of(half * rows, rows), rows)


class _Exchange:
    def __init__(self, inputs, out_shape, n_sems, start, finish=None, middle=None, rest=None):
        self.inputs, self.out_shape, self.n_sems, self.start = inputs, out_shape, n_sems, start
        if finish is None:
            def finish(*refs):
                middle(*refs)
                rest(*refs)
        self.finish = finish
        self.middle = middle if middle is not None else (lambda *refs: None)
        self.rest = rest if rest is not None else finish

    def sem_shapes(self):
        return [pltpu.SemaphoreType.DMA((self.n_sems,)), pltpu.SemaphoreType.DMA((self.n_sems,))]

    def split(self, refs):
        n_in, n_out = len(self.inputs), len(self.out_shape)
        return refs[:n_in], refs[n_in:n_in + n_out], refs[n_in + n_out], refs[n_in + n_out + 1]


def _allgather_exchange(shards, whole=(), sent_rows=None):
    n, nw = len(shards), len(whole)
    slots = 8
    sent_rows = [s.shape[0] for s in shards] if sent_rows is None else sent_rows

    def plan(c, src, outs, send_sems, recv_sems):
        x, y, _ = _place()
        via_x, via_y, diagonal = _other_chips(x, y)
        id_x, id_y, id_diagonal = [2 * chip[0] + chip[1] for chip in (via_x, via_y, diagonal)]
        me, sibling = (x, y, c), (x, y, 1 - c)

        def rows_of(a, half, quarter):
            total = src[a].shape[0]
            first = half * (total // 2) + (0 if quarter is None else quarter * (total // 4))
            size = min(total // 2 if quarter is None else total // 4, sent_rows[a] - first)
            assert size > 0 and size % BF16_ROWS == 0, (a, half, quarter, size)
            return pl.ds(first, size)

        def copy(a, k, chip_index, half, quarter, to, from_src=False):
            rows = rows_of(a, half, quarter)
            dst = outs[a].at[chip_index, rows]
            return pltpu.make_async_remote_copy(
                src_ref=src[a].at[rows] if from_src else dst, dst_ref=dst, send_sem=send_sems.at[slots * a + k],
                recv_sem=recv_sems.at[slots * a + k], device_id=to, device_id_type=MESH)

        def whole_copy(b, k, chip_index, to):
            return pltpu.make_async_remote_copy(
                src_ref=src[n + b], dst_ref=outs[n + b].at[chip_index], send_sem=send_sems.at[slots * n + 3 * b + k],
                recv_sem=recv_sems.at[slots * n + 3 * b + k], device_id=to, device_id_type=MESH)

        first, stages, last = [], [], []
        for a in range(n):
            first += [copy(a, 0, 2 * x + y, c, None, (*via_x, c), True), copy(a, 1, 2 * x + y, c, None, (*via_y, c), True)]
            stages.append([
                (copy(a, 0, id_x, c, None, me),
                 [copy(a, 2, id_x, c, 0, (*via_y, c)), copy(a, 4, id_x, c, None, sibling)]),
                (copy(a, 1, id_y, c, None, me),
                 [copy(a, 3, id_y, c, 1, (*via_x, c)), copy(a, 5, id_y, c, None, sibling)]),
                (copy(a, 2, id_diagonal, c, 0, me), [copy(a, 6, id_diagonal, c, 0, sibling)]),
                (copy(a, 3, id_diagonal, c, 1, me), [copy(a, 7, id_diagonal, c, 1, sibling)]),
            ])
            last += [copy(a, 4, id_x, 1 - c, None, me), copy(a, 5, id_y, 1 - c, None, me),
                     copy(a, 6, id_diagonal, 1 - c, 0, me), copy(a, 7, id_diagonal, 1 - c, 1, me)]
        for b in range(nw):
            for k, (chip, index) in enumerate(((via_x, id_x), (via_y, id_y), (diagonal, id_diagonal))):
                first.append(whole_copy(b, k, 2 * x + y, (*chip, c)))
                last.append(whole_copy(b, k, index, me))
        return first, stages, last

    def each_core(step):
        def run(*refs):
            for core in (0, 1):
                @pl.when(lax.axis_index("c") == core)
                def _(core=core):
                    step(core, *refs)

        return run

    def start(core, *refs):
        for cp in plan(core, *refs)[0]:
            cp.start()

    def pass_on(stages, which):
        for stage in which:
            for per_shard in stages:
                lands, onward = per_shard[stage]
                lands.wait_recv()
                for cp in onward:
                    cp.start()

    def middle(core, *refs):
        pass_on(plan(core, *refs)[1], (0, 1))

    def rest(core, *refs):
        first, stages, last = plan(core, *refs)
        pass_on(stages, (2, 3))
        for cp in last:
            cp.wait_recv()
        for cp in first + [cp for per_shard in stages for _, onward in per_shard for cp in onward]:
            cp.wait_send()

    out_shape = [jax.ShapeDtypeStruct((NCHIP,) + s.shape, s.dtype) for s in list(shards) + list(whole)]
    return _Exchange(list(shards) + list(whole), out_shape, slots * n + 3 * nw, each_core(start),
                     middle=each_core(middle), rest=each_core(rest))


def _with_own(gathered, own):
    x, y, _ = _place()
    return lax.dynamic_update_index_in_dim(gathered, own, 2 * x + y, axis=0)


def _simple_exchange(inputs, out_shape, copies_of):
    def start(*refs):
        for cp in copies_of(*refs):
            cp.start()

    def finish(*refs):
        for cp in copies_of(*refs):
            cp.wait()

    return _Exchange(list(inputs), out_shape, len(out_shape) * 3, start, finish)


def _pair_exchange(grads):
    def copies_of(src, outs, send_sems, recv_sems):
        x, y, c = _place()
        return [pltpu.make_async_remote_copy(
            src_ref=src[a].at[:, _half_rows(src[a].shape[1], 1 - c)], dst_ref=outs[a], send_sem=send_sems.at[a],
            recv_sem=recv_sems.at[a], device_id=(x, y, 1 - c), device_id_type=MESH) for a in range(len(src))]

    return _simple_exchange(
        grads, [jax.ShapeDtypeStruct((g.shape[0], g.shape[1] // 2, g.shape[2]), g.dtype) for g in grads], copies_of)


def _pair_sum(grads, theirs, name):
    n = len(grads)

    def body(place_ref, *refs):
        for a in range(n):
            refs[2 * n + a][...] = (refs[a][...].astype(F32) + refs[n + a][...].astype(F32)).astype(BF16)

    def specs(arrs):
        return [pl.BlockSpec((None,) + g.shape[1:], lambda j, place: (j, 0, 0)) for g in arrs]

    own_half = [pl.BlockSpec((None, g.shape[1] // 2, g.shape[2]), lambda j, place: (j, place[1], 0)) for g in grads]
    return pl.pallas_call(
        body, name=name, grid_spec=pltpu.PrefetchScalarGridSpec(
            num_scalar_prefetch=1, grid=(NCHIP,), in_specs=own_half + specs(theirs), out_specs=specs(theirs)),
        out_shape=[jax.ShapeDtypeStruct(g.shape, BF16) for g in theirs], compiler_params=_cparams(),
    )(_place_scalars(), *grads, *theirs)


def _chip_exchange(parts, rows_by_core=None):
    if rows_by_core is None:
        rows_by_core = [(p.shape[1],) * 2 for p in parts]

    def each_core(act):
        def run(src, outs, send_sems, recv_sems):
            x, y, c = _place()
            for core in (0, 1):
                @pl.when(c == core)
                def _(core=core):
                    for a in range(len(src)):
                        rows = pl.ds(0, rows_by_core[a][core])
                        for k, chip in enumerate(_other_chips(x, y)):
                            act(pltpu.make_async_remote_copy(
                                src_ref=src[a].at[2 * chip[0] + chip[1], rows], dst_ref=outs[a].at[k, rows],
                                send_sem=send_sems.at[3 * a + k], recv_sem=recv_sems.at[3 * a + k],
                                device_id=(*chip, core), device_id_type=MESH))

        return run

    return _Exchange(list(parts), [jax.ShapeDtypeStruct((NCHIP - 1,) + p.shape[1:], p.dtype) for p in parts],
                     3 * len(parts), each_core(lambda cp: cp.start()), each_core(lambda cp: cp.wait()))


def _chip_sum(parts, received, exchange=None):
    n = len(parts)
    steps = 4
    ex_in, ex_in_specs, ex_out_specs, ex_out_shape, ex_scratch = _hosted(exchange)
    n_ex = len(ex_in)

    def body(place_ref, *refs):
        mine, theirs = refs[2 * n + n_ex:3 * n + n_ex], refs[3 * n + n_ex:4 * n + n_ex]
        ex_refs = refs[2 * n:2 * n + n_ex] + refs[4 * n + n_ex:len(refs) - n - 2]
        tiles, send_sems, recv_sems = refs[len(refs) - n - 2:len(refs) - 2], refs[-2], refs[-1]
        step = pl.program_id(0)
        if exchange is not None:
            @pl.when(step == 0)
            def _():
                exchange.start(*exchange.split(ex_refs))

        x, y, c = _place()

        def share(a, i):
            rows = tiles[a].shape[1]
            return pltpu.make_async_remote_copy(
                src_ref=tiles[a].at[i], dst_ref=theirs[a].at[pl.ds(pl.multiple_of(i * rows, rows), rows)],
                send_sem=send_sems.at[a * steps + i], recv_sem=recv_sems.at[a * steps + i], device_id=(x, y, 1 - c),
                device_id_type=MESH)

        for a in range(n):
            own, r = refs[a], refs[n + a]
            total = ((own[...].astype(F32) + r[0].astype(F32)) + r[1].astype(F32)) + r[2].astype(F32)
            mine[a][...] = total
            tiles[a][step] = total
            share(a, step).start()

        @pl.when(step == steps - 1)
        def _():
            if exchange is not None:
                exchange.finish(*exchange.split(ex_refs))
            for a in range(n):
                for i in range(steps):
                    share(a, i).wait()

    own_specs = [pl.BlockSpec((None, g.shape[1] // steps, g.shape[2]), lambda i, place: (place[0], i, 0)) for g in parts]
    received_specs = [pl.BlockSpec((g.shape[0], g.shape[1] // steps, g.shape[2]), lambda i, place: (0, i, 0))
                      for g in received]
    out_specs = [pl.BlockSpec((g.shape[1] // steps, g.shape[2]), lambda i, place: (i, 0)) for g in parts]
    halves = [jax.ShapeDtypeStruct(g.shape[1:], F32) for g in parts]
    res = pl.pallas_call(
        body, name="grads_chip_sum", grid_spec=pltpu.PrefetchScalarGridSpec(
            num_scalar_prefetch=1, grid=(steps,), in_specs=own_specs + received_specs + ex_in_specs,
            out_specs=out_specs + [HBM] * n + ex_out_specs,
            scratch_shapes=ex_scratch + [pltpu.VMEM((steps, g.shape[1] // steps, g.shape[2]), F32) for g in parts]
            + [pltpu.SemaphoreType.DMA((n * steps,))] * 2),
        out_shape=halves * 2 + ex_out_shape, compiler_params=_cparams(),
    )(_place_scalars(), *parts, *received, *ex_in)
    return res[:n], res[n:2 * n], res[2 * n:]


def _adamw_math(w, g, m, v):
    nm = ADAM_B1 * m + (1.0 - ADAM_B1) * g
    nv = ADAM_B2 * v + (1.0 - ADAM_B2) * jnp.square(g)
    m_hat = nm / (1.0 - ADAM_B1 ** ADAM_STEP)
    v_hat = nv / (1.0 - ADAM_B2 ** ADAM_STEP)
    return -ADAM_LR * (m_hat / (jnp.sqrt(v_hat) + ADAM_EPS) + ADAM_WD * w), nm, nv


def _adamw_big(ws, g_mine, g_theirs, ms, vs):
    n = len(ws)
    steps = 8
    per_half = steps // 2

    def body(place_ref, *refs):
        outs = refs[5 * n:]
        own_half = (pl.program_id(0) // per_half) == place_ref[1]
        for a in range(n):
            g = jnp.where(own_half, refs[n + a][...], refs[2 * n + a][...])
            d, nm, nv = _adamw_math(refs[a][...], g, refs[3 * n + a][...], refs[4 * n + a][...])
            outs[a][...] = g
            outs[n + a][...] = d
            outs[2 * n + a][...] = nm
            outs[3 * n + a][...] = nv

    specs = [pl.BlockSpec((w.shape[0] // steps, w.shape[1]), lambda i, place: (i, 0)) for w in ws]

    def half_specs(halves, of_this_core):
        def tile(i, place):
            first = (place[1] if of_this_core else 1 - place[1]) * per_half
            return jnp.clip(i - first, 0, per_half - 1), 0

        return [pl.BlockSpec((g.shape[0] // per_half, g.shape[1]), tile) for g in halves]

    shapes = [jax.ShapeDtypeStruct(w.shape, F32) for w in ws]
    res = pl.pallas_call(
        body, name="adamw_big", grid_spec=pltpu.PrefetchScalarGridSpec(
            num_scalar_prefetch=1, grid=(steps,),
            in_specs=specs + half_specs(g_mine, True) + half_specs(g_theirs, False) + specs * 2, out_specs=specs * 4),
        out_shape=shapes * 4, compiler_params=_cparams(),
    )(_place_scalars(), *ws, *g_mine, *g_theirs, *ms, *vs)
    return res[:n], res[n:2 * n], res[2 * n:3 * n], res[3 * n:]


def _adamw_in(w, g_mine, g_theirs, m, v):
    half = D // 2

    def body(w_ref, gm_ref, gt_ref, m_ref, v_ref, g_out, d_out, nm_out, nv_out, g_ref):
        south = lax.axis_index("c") == 0
        g_ref[0:half, :] = jnp.where(south, gm_ref[...], gt_ref[...])
        g_ref[half:D, :] = jnp.where(south, gt_ref[...], gm_ref[...])
        g = g_ref[0:CW, :]
        d, nm, nv = _adamw_math(w_ref[...], g, m_ref[...], v_ref[...])
        g_out[...] = g
        d_out[...] = d
        nm_out[...] = nm
        nv_out[...] = nv

    cols = 2 * LANES
    spec = pl.BlockSpec((CW, cols), lambda i: (0, i))
    half_spec = pl.BlockSpec((half, cols), lambda i: (0, i))
    return pl.pallas_call(
        body, name="adamw_in", grid=(D // cols,), in_specs=[spec, half_spec, half_spec, spec, spec], out_specs=[spec] * 4,
        out_shape=[jax.ShapeDtypeStruct((CW, D), F32)] * 4, scratch_shapes=[pltpu.VMEM((D, cols), F32)],
        compiler_params=_cparams(),
    )(w, g_mine, g_theirs, m, v)


NORM_NAMES = ("pre_mix_norm", "post_mix_norm", "pre_mlp_norm", "post_mlp_norm")
SMALL_NAMES = NORM_NAMES + ("gdn_conv_w", "fox_f_bias", "gdn_dt_bias", "gdn_a_log", "fox_out_norm", "gdn_out_norm")
CONV_COLS = 3 * DGDN // NCHIP


def _small_gather(d_norms, d_conv, sums, d_fox_norm, d_gdn_norm, loss_row):
    n_arrays = 6
    n_remote = n_arrays * (NDEV - 1)

    def copies_of(src, outs, send_sems, recv_sems):
        x, y, c = _place()
        me = 4 * x + 2 * y + c

        def from_me(chip_index):
            cols = pl.ds(pl.multiple_of(chip_index * CONV_COLS, LANES), CONV_COLS)
            return [src[0], src[1].at[:, cols], src[2], src[3], src[4], src[5]]

        local = [pltpu.make_async_copy(s, outs[a].at[me], send_sems.at[n_remote + a]) for a, s in enumerate(from_me(2 * x + y))]
        remote = []
        for k in range(1, NDEV):
            px, py, pc = x ^ ((k >> 2) & 1), y ^ ((k >> 1) & 1), c ^ (k & 1)
            remote += [pltpu.make_async_remote_copy(
                src_ref=s, dst_ref=outs[a].at[me], send_sem=send_sems.at[n_arrays * (k - 1) + a],
                recv_sem=recv_sems.at[n_arrays * (k - 1) + a], device_id=(px, py, pc), device_id_type=MESH)
                for a, s in enumerate(from_me(2 * px + py))]
        return local + remote

    def start(*refs):
        for cp in copies_of(*refs):
            cp.start()

    def finish(*refs):
        for cp in copies_of(*refs):
            cp.wait()

    shapes = [(4, D), (CONV_K, CONV_COLS), (8, LANES), (1, LANES), (1, LANES), (1, LANES)]
    return _Exchange([d_norms, d_conv, sums, d_fox_norm, d_gdn_norm, loss_row],
                     [jax.ShapeDtypeStruct((NDEV,) + s, F32) for s in shapes], n_remote + n_arrays, start, finish)


def _small_adamw(gathered, ws, ms, vs):
    n = len(SMALL_NAMES)
    ng = len(gathered)

    def body(*refs):
        def total(buf):
            acc = buf[0]
            for i in range(1, NDEV):
                acc = acc + buf[i]
            return acc

        t_norms, t_conv, t_sums, t_fn, t_gn, t_loss = [total(r) for r in refs[:ng]]
        w_refs, m_refs, v_refs = refs[ng:ng + n], refs[ng + n:ng + 2 * n], refs[ng + 2 * n:ng + 3 * n]
        outs = refs[ng + 3 * n:]
        outs[4 * n][...] = t_loss
        grads = [t_norms[i:i + 1, :] for i in range(4)] + [
            t_conv, t_sums[0:1, 0:NFH], t_sums[1:2, 0:NGH], t_sums[2:3, 0:NGH], t_fn[:, 0:FHD], t_gn]
        for a in range(n):
            d, nm, nv = _adamw_math(w_refs[a][...], grads[a], m_refs[a][...], v_refs[a][...])
            outs[a][...] = grads[a]
            outs[n + a][...] = d
            outs[2 * n + a][...] = nm
            outs[3 * n + a][...] = nv

    def whole(arr):
        return pl.BlockSpec(arr.shape, lambda i: (0,) * arr.ndim)

    res = pl.pallas_call(
        body, name="small_adamw", grid=(1,), in_specs=[whole(t) for t in gathered] + [whole(w) for w in ws] * 3,
        out_specs=[whole(w) for w in ws] * 4 + [pl.BlockSpec((1, LANES), lambda i: (0, 0))],
        out_shape=[jax.ShapeDtypeStruct(w.shape, F32) for w in ws] * 4 + [jax.ShapeDtypeStruct((1, LANES), F32)],
        compiler_params=_cparams(),
    )(*gathered, *ws, *ms, *vs)
    return res[:n], res[n:2 * n], res[2 * n:3 * n], res[3 * n:4 * n], res[4 * n]


CW = DPROJ // NCHIP
BF16_ROWS = 16
CW_SENT = -(-CW // BF16_ROWS) * BF16_ROWS
PROJ_RUNS = tuple((part * DFOX + hp * LANES, part * DFOX + (hp + 1) * LANES, (3 * hp + part) * LANES)
                  for hp in range(NPAIR) for part in range(3)) + (
    (1536, 1544, BLK_SMALL * LANES), (1544, 3080, BLK_GDN * LANES), (3080, 3088, BLK_SMALL * LANES + 8),
    (3088, 3600, BLK_GZ * LANES))


def _proj_pieces():
    pieces = []
    for lo, hi, at in PROJ_RUNS:
        while lo < hi:
            j = lo // CW
            end = min(hi, (j + 1) * CW)
            pieces.append((j, lo - j * CW, at, end - lo))
            at, lo = at + end - lo, end
    return pieces


RT = 256


def _to_padded_rows(gathered):
    def body(src_ref, out_ref, blocks_ref, rows_ref):
        blocks_ref[...] = src_ref[...].astype(F32)
        rows_ref[...] = jnp.zeros_like(rows_ref)
        for j, start, at, n in _proj_pieces():
            rows_ref[at:at + n, :] = blocks_ref[j, start:start + n, :]
        out_ref[...] = rows_ref[...].astype(out_ref.dtype)

    return pl.pallas_call(
        body, name="proj_rows_in", grid=(D // RT,), in_specs=[pl.BlockSpec((NCHIP, D, RT), lambda i: (0, 0, i))],
        out_specs=pl.BlockSpec((DPROJ_PAD, RT), lambda i: (0, i)), out_shape=jax.ShapeDtypeStruct((DPROJ_PAD, D), gathered.dtype),
        scratch_shapes=[pltpu.VMEM((NCHIP, D, RT), F32), pltpu.VMEM((DPROJ_PAD, RT), F32)], compiler_params=_cparams(),
    )(gathered)


def _from_padded_rows_pair_sum(w):
    steps = D // RT
    half = D // 2

    def body(src_ref, out_ref, rows_ref, blocks_ref, mine_ref, send_ref, recv_ref, send_sems, recv_sems):
        i = pl.program_id(0)
        x, y, c = _place()

        def share(t):
            return pltpu.make_async_remote_copy(
                src_ref=send_ref.at[t], dst_ref=recv_ref.at[t], send_sem=send_sems.at[t], recv_sem=recv_sems.at[t],
                device_id=(x, y, 1 - c), device_id_type=MESH)

        def rows_of(core):
            return blocks_ref[:, pl.ds(pl.multiple_of(core * half, half), half), :]

        @pl.when(i < steps)
        def _():
            rows_ref[...] = src_ref[...].astype(F32)
            blocks_ref[...] = jnp.zeros_like(blocks_ref)
            for j, start, at, n in _proj_pieces():
                blocks_ref[j, start:start + n, :] = rows_ref[at:at + n, :]
            mine_ref[i % 2] = rows_of(c)
            send_ref[i] = rows_of(1 - c).astype(BF16)
            share(i).start()

        @pl.when(i > 0)
        def _():
            share(i - 1).wait_recv()
            out_ref[...] = (mine_ref[(i - 1) % 2] + recv_ref[i - 1].astype(F32)).astype(BF16)

        @pl.when(i == steps)
        def _():
            for t in range(steps):
                share(t).wait_send()

    tile = (NCHIP, half, RT)
    return pl.pallas_call(
        body, name="proj_rows_out_pair_sum", grid=(steps + 1,),
        in_specs=[pl.BlockSpec((DPROJ_PAD, RT), lambda i: (0, jnp.minimum(i, steps - 1)))],
        out_specs=pl.BlockSpec(tile, lambda i: (0, 0, jnp.maximum(i - 1, 0))),
        out_shape=jax.ShapeDtypeStruct((NCHIP, half, D), BF16),
        scratch_shapes=[pltpu.VMEM((DPROJ_PAD, RT), F32), pltpu.VMEM((NCHIP, D, RT), F32), pltpu.VMEM((2,) + tile, F32),
                        pltpu.VMEM((steps,) + tile, BF16), pltpu.VMEM((steps,) + tile, BF16),
                        pltpu.SemaphoreType.DMA((steps,)), pltpu.SemaphoreType.DMA((steps,))],
        compiler_params=_cparams(),
    )(w)


def _local_step(x, target, first_weights, late_weights, reduce_late, reduce_in, pre_mix_norm, fox_f_bias, fox_out_norm,
                gdn_a_log, gdn_dt_bias, gdn_out_norm, post_mix_norm, pre_mlp_norm, post_mlp_norm):
    bias_vec = jnp.zeros((1, LANES), F32).at[0, 0:NFH].set(fox_f_bias).at[0, LANE_G:LANE_G + NGH].set(gdn_dt_bias)
    alog_vec = jnp.zeros((1, LANES), F32).at[0, LANE_G:LANE_G + NGH].set(gdn_a_log)
    w2 = jnp.concatenate([fox_out_norm, fox_out_norm], axis=1)

    h, first = _pre_norm(x, pre_mix_norm, exchange=first_weights[0])
    win_p, conv_w = first_weights[1](first)
    proj = _matmul(h, win_p, tb=True, tm=2048, tn=768, tk=1024, name="mm_proj", exchange=late_weights[0])
    proj, late_a = proj if late_weights[0] is not None else (proj, [])
    gates = _gates(proj, bias_vec, alog_vec)
    mix, fox_o, lse, late_b = _fox_fwd(proj, gates, w2, exchange=late_weights[1])
    qkv = _gdn_pre(proj, conv_w)
    (u, w, qd, kd, a_intra, gl, t_inv), _ = _gdn_prep(qkv, gates)
    wout, wup3 = late_weights[3](late_a, late_b)
    mix, gdn_raw, states = _gdn_scan(u, w, qd, kd, a_intra, gl, proj, gdn_out_norm, mix)
    def post_mix(acc, xv, w_post, w_pre_mlp):
        x1v = xv + acc * _rms_scale(acc) * w_post
        return acc, x1v, x1v * _rms_scale(x1v) * w_pre_mlp

    mixed, x1, h2 = _matmul(mix, wout, tm=512, tn=D, tk=1024, out_dtypes=(F32, F32, BF16), name="mm_out",
                            extra=(x, post_mix_norm, pre_mlp_norm), epilogue=post_mix)

    def relu2(acc):
        r = jnp.maximum(acc, 0.0)
        return r, r * r

    up_act = _matmul(h2, wup3, b3=True, tm=1024, tn=1024, tk=1024, out_dtypes=(BF16, BF16), epilogue=relu2,
                     name="mm_up", exchange=late_weights[2])
    (up_relu, act), late_c = up_act if late_weights[2] is not None else (up_act, [])
    wdown = late_weights[4](late_c)
    def loss_head(acc, x1v, tv, w):
        err = x1v + acc * _rms_scale(acc) * w - tv
        dx2v = err * (1.0 / D)
        dyv, dwt = _rms_bwd(acc, w, dx2v)
        part = 0.5 * jnp.sum(jnp.mean(err * err, axis=-1, keepdims=True), axis=0, keepdims=True)
        return dx2v, dyv, jnp.sum(dwt, axis=0, keepdims=True), jnp.broadcast_to(part, (1, D))

    dx2, dy, d_post_mlp, loss_wide = _matmul(
        act, wdown, tm=512, tn=D, tk=DFF, out_dtypes=(F32, BF16, F32, F32), extra=(x1, target, post_mlp_norm),
        epilogue=loss_head, n_sums=2, name="mm_down")
    loss_row = loss_wide[:, :LANES]

    dwdown = _matmul(act, dy, ta=True, tm=1024, tn=1024, tk=2048, out_dtypes=(BF16,), name="mm_dwdown")

    def relu2_bwd(acc, r):
        return (acc * 2.0 * r.astype(F32),)

    dup = _matmul(dy, wdown, tb=True, tm=1024, tn=1024, tk=1024, out_dtypes=(BF16,), extra=(up_relu,), epilogue=relu2_bwd,
                  name="mm_dact")
    dwup3 = _matmul(h2, dup, ta=True, tm=1024, tn=1024, tk=2048, out_dtypes=(BF16,), o3=True, name="mm_dwup")
    def mid_bwd(acc, x1v, dx2v, mixedv, w_pre_mlp, w_post):
        dxa, dwm = _rms_bwd(x1v, w_pre_mlp, acc)
        dx1v = dx2v + dxa
        dm, dwp = _rms_bwd(mixedv, w_post, dx1v)
        return dx1v, dm, jnp.sum(dwm, axis=0, keepdims=True), jnp.sum(dwp, axis=0, keepdims=True)

    dx1, dmixed, d_pre_mlp, d_post_mix = _matmul(
        dup, wup3, tb=True, b3=True, tm=512, tn=D, tk=DFF, out_dtypes=(F32, BF16, F32, F32),
        extra=(x1, dx2, mixed, pre_mlp_norm, post_mix_norm), epilogue=mid_bwd, n_sums=2, name="mm_dh2")
    dwout = _matmul(mix, dmixed, ta=True, tm=256, tn=1024, tk=2048, out_dtypes=(BF16,), name="mm_dwout")
    dmix = _matmul(dmixed, wout, tb=True, tm=512, tn=1024, tk=1024, name="mm_dmix")

    dfox, delta, d_fox_norm, from_sibling = _fox_norm_bwd(fox_o, dmix, w2, exchange=reduce_late[0](dwout, dwup3, dwdown))
    dproj, dcum_fox, reduced_a = _fox_bwd(proj, dfox, gates, lse, delta, exchange=reduce_late[1](from_sibling))
    (dproj, du, dw, dqd, dkd, da, dgl, d_gdn_norm), reduced_b = _gdn_scan_bwd(
        dmix, gdn_raw, proj, gdn_out_norm, u, w, qd, kd, a_intra, gl, states, dproj, exchange=reduce_late[2]())
    dqkv, dgates_gdn, reduced_c = _gdn_prep_bwd(qkv, gates, t_inv, du, dw, dqd, dkd, da, dgl, exchange=reduce_late[3]())
    reduced_late = (reduced_a, reduced_b, reduced_c)
    dproj, d_conv = _gdn_pre_bwd(proj, conv_w, dqkv, dproj)
    dproj, sums = _gates_bwd(proj, bias_vec, alog_vec, dgates_gdn, dcum_fox, dproj)

    dwin_p = _matmul(dproj, h, ta=True, tm=1280, tn=1024, tk=2048, out_dtypes=(BF16,), name="mm_dwin")
    exchange_in = reduce_in(dwin_p)
    dh = _matmul(dproj, win_p, tm=1024, tk=DPROJ_PAD, name="mm_dh", exchange=exchange_in)
    dh, reduced_in = dh if exchange_in is not None else (dh, [])
    grad_x, d_pre_mix = _pre_norm_bwd(dh, x, pre_mix_norm, dx1)

    d_norms = jnp.concatenate([d_pre_mix, d_post_mix, d_pre_mlp, d_post_mlp], axis=0)
    return grad_x, (d_norms, d_conv, sums, d_fox_norm, d_gdn_norm, loss_row), reduced_late, reduced_in


def kernel(x, pre_mix_norm, w_in, fox_f_bias, fox_out_norm, gdn_conv_w, gdn_a_log, gdn_dt_bias, gdn_out_norm, w_out, post_mix_norm, pre_mlp_norm, w_up, w_down, post_mlp_norm, loss_target, m_pre_mix_norm, m_w_in, m_fox_f_bias, m_fox_out_norm, m_gdn_conv_w, m_gdn_a_log, m_gdn_dt_bias, m_gdn_out_norm, m_w_out, m_post_mix_norm, m_pre_mlp_norm, m_w_up, m_w_down, m_post_mlp_norm, v_pre_mix_norm, v_w_in, v_fox_f_bias, v_fox_out_norm, v_gdn_conv_w, v_gdn_a_log, v_gdn_dt_bias, v_gdn_out_norm, v_w_out, v_post_mix_norm, v_pre_mlp_norm, v_w_up, v_w_down, v_post_mlp_norm):
    weights = dict(pre_mix_norm=pre_mix_norm, w_in=w_in, fox_f_bias=fox_f_bias, fox_out_norm=fox_out_norm, gdn_conv_w=gdn_conv_w,
                   gdn_a_log=gdn_a_log, gdn_dt_bias=gdn_dt_bias, gdn_out_norm=gdn_out_norm, w_out=w_out, post_mix_norm=post_mix_norm,
                   pre_mlp_norm=pre_mlp_norm, w_up=w_up, w_down=w_down, post_mlp_norm=post_mlp_norm)
    m_in = dict(pre_mix_norm=m_pre_mix_norm, w_in=m_w_in, fox_f_bias=m_fox_f_bias, fox_out_norm=m_fox_out_norm, gdn_conv_w=m_gdn_conv_w,
                gdn_a_log=m_gdn_a_log, gdn_dt_bias=m_gdn_dt_bias, gdn_out_norm=m_gdn_out_norm, w_out=m_w_out, post_mix_norm=m_post_mix_norm,
                pre_mlp_norm=m_pre_mlp_norm, w_up=m_w_up, w_down=m_w_down, post_mlp_norm=m_post_mlp_norm)
    v_in = dict(pre_mix_norm=v_pre_mix_norm, w_in=v_w_in, fox_f_bias=v_fox_f_bias, fox_out_norm=v_fox_out_norm, gdn_conv_w=v_gdn_conv_w,
                gdn_a_log=v_gdn_a_log, gdn_dt_bias=v_gdn_dt_bias, gdn_out_norm=v_gdn_out_norm, w_out=v_w_out, post_mix_norm=v_post_mix_norm,
                pre_mlp_norm=v_pre_mlp_norm, w_up=v_w_up, w_down=v_w_down, post_mlp_norm=v_post_mlp_norm)
    order_w = ("pre_mix_norm", "w_in", "fox_f_bias", "fox_out_norm", "gdn_conv_w", "gdn_a_log", "gdn_dt_bias", "gdn_out_norm", "w_out",
               "post_mix_norm", "pre_mlp_norm", "w_up", "w_down", "post_mlp_norm")
    big = ("w_in", "w_out", "w_up", "w_down")

    def row(v):
        return v if v.ndim == 2 else v.reshape(1, -1)

    win_shard = jnp.pad(w_in.T.astype(BF16), ((0, D - CW), (0, 0)))

    def resolve_first(gathered):
        win_g, conv_g = gathered
        return (_to_padded_rows(_with_own(win_g, win_shard)),
                _with_own(conv_g, gdn_conv_w).transpose(1, 0, 2).reshape(CONV_K, 3 * DGDN))

    late_shards = [weights[n].astype(BF16) for n in big[1:]]

    gathered_down = []

    def resolve_out_up(gathered_out, gathered_mlp):
        gathered_down.append(gathered_mlp[1])
        return _with_own(gathered_out[0], late_shards[0]).reshape(D, D), _with_own(gathered_mlp[0], late_shards[1])

    def resolve_down(_):
        return _with_own(gathered_down[0], late_shards[2]).reshape(DFF, D)

    pair_sums, late_blocks = {}, []

    def pair_summed(names, blocks, theirs):
        for n, s in zip(names, _pair_sum(blocks, theirs, "grads_pair_sum_" + names[0])):
            pair_sums[n] = s

    def late_pair_exchange(dwout, dwup3, dwdown):
        late_blocks.extend([dwout.reshape(NCHIP, D // NCHIP, D), dwup3, dwdown.reshape(NCHIP, DFF // NCHIP, D)])
        return _pair_exchange(late_blocks)

    def late_chip_exchange(theirs):
        pair_summed(big[1:], late_blocks, theirs)
        return _chip_exchange([pair_sums["w_up"], pair_sums["w_down"]])

    def reduce_in(dwin_p):
        pair_sums["w_in"] = _from_padded_rows_pair_sum(dwin_p)
        return _chip_exchange([pair_sums["w_in"]], rows_by_core=[(D // 2, CW_SENT - D // 2)])

    grad_x, small, received_late, received_in = _local_step(
        x[0], loss_target[0], (_allgather_exchange([win_shard], whole=[gdn_conv_w], sent_rows=[CW_SENT]), resolve_first),
        (_allgather_exchange(late_shards[:1]), _allgather_exchange(late_shards[1:]), None, resolve_out_up, resolve_down),
        (late_pair_exchange, late_chip_exchange, lambda: None, lambda: _chip_exchange([pair_sums["w_out"]])),
        reduce_in, row(pre_mix_norm), fox_f_bias, row(fox_out_norm), gdn_a_log, gdn_dt_bias,
        row(gdn_out_norm), row(post_mix_norm), row(pre_mlp_norm), row(post_mlp_norm))
    received_mlp, _, received_out = received_late

    g_mine, g_theirs, small_gathered = _chip_sum(
        [pair_sums[n] for n in big], list(received_in[:1]) + list(received_out[:1]) + list(received_mlp[:2]),
        exchange=_small_gather(*small))

    g_big, d_big, nm_big, nv_big = _adamw_big(
        [weights[n] for n in big[1:]], g_mine[1:], g_theirs[1:], [m_in[n] for n in big[1:]], [v_in[n] for n in big[1:]])
    in_t = _adamw_in(w_in.T, g_mine[0], g_theirs[0], m_w_in.T, v_w_in.T)
    g_small, d_small, nm_small, nv_small, loss_total = _small_adamw(
        small_gathered, [row(weights[n]) for n in SMALL_NAMES], [row(m_in[n]) for n in SMALL_NAMES],
        [row(v_in[n]) for n in SMALL_NAMES])

    grads, delta, new_m, new_v = {}, {}, {}, {}
    grads["w_in"], delta["w_in"], new_m["w_in"], new_v["w_in"] = [t.T for t in in_t]
    for i, n in enumerate(big[1:]):
        grads[n], delta[n], new_m[n], new_v[n] = g_big[i], d_big[i], nm_big[i], nv_big[i]
    for i, n in enumerate(SMALL_NAMES):
        shape = weights[n].shape
        grads[n], delta[n], new_m[n], new_v[n] = (g_small[i].reshape(shape), d_small[i].reshape(shape),
                                                  nm_small[i].reshape(shape), nv_small[i].reshape(shape))
    return (loss_total[0, 0], grad_x[None], *[grads[n] for n in order_w], *[delta[n] for n in order_w], *[new_m[n] for n in order_w],
            *[new_v[n] for n in order_w])
```

```python
import jax
import jax.numpy as jnp
from jax import lax
from jax.experimental import pallas as pl
from jax.experimental.pallas import tpu as pltpu

F32 = jnp.float32
BF16 = jnp.bfloat16
MESH = pl.DeviceIdType.MESH

S = 2048
D = 1024
NFH, FHD = 8, 64
NPAIR = NFH // 2
NGH, GHD = 4, 128
DFOX = NFH * FHD
DGDN = NGH * GHD
CHUNK = 64
NCH = S // CHUNK
CONV_K = 4
DFF = 4 * D
EPS = 1e-6
DPROJ = 3600
LANES = 128
DPROJ_PAD = 3840
BLK_GDN = 12
BLK_GZ = 24
BLK_SMALL = 28
NCHIP = 4
NDEV = 8
VMEM_LIMIT = 56 * 1024 * 1024

ADAM_LR = 0.001
ADAM_B1 = 0.9
ADAM_B2 = 0.999
ADAM_EPS = 1e-08
ADAM_WD = 0.01
ADAM_STEP = 10


def _cparams(**kw):
    return pltpu.CompilerParams(vmem_limit_bytes=VMEM_LIMIT, **kw)


def _dn(ca, cb):
    return (((ca,), (cb,)), ((), ()))


def _dot(a, b, ca=1, cb=0):
    return lax.dot_general(a.astype(BF16), b.astype(BF16), _dn(ca, cb), preferred_element_type=F32)


def _hdot(a, b, ca=1, cb=0):
    return lax.dot_general(a.astype(F32), b.astype(F32), _dn(ca, cb), precision=lax.Precision.HIGHEST,
                           preferred_element_type=F32)


def _dot3(a, b, ca=1, cb=0):
    a_hi, b_hi = a.astype(BF16), b.astype(BF16)
    a_lo, b_lo = (a - a_hi.astype(F32)).astype(BF16), (b - b_hi.astype(F32)).astype(BF16)
    dn = _dn(ca, cb)
    return (lax.dot_general(a_hi, b_hi, dn, preferred_element_type=F32)
            + (lax.dot_general(a_hi, b_lo, dn, preferred_element_type=F32)
               + lax.dot_general(a_lo, b_hi, dn, preferred_element_type=F32)))


@jax.custom_vjp
def _mm_nn(a, b):
    return _dot(a, b, 1, 0)


def _mm_nn_fwd(a, b):
    return _dot(a, b, 1, 0), (a, b)


def _mm_nn_bwd(res, g):
    a, b = res
    return _dot(g, b, 1, 1), _dot(a, g, 0, 0)


_mm_nn.defvjp(_mm_nn_fwd, _mm_nn_bwd)


@jax.custom_vjp
def _mm_nt(a, b):
    return _dot(a, b, 1, 1)


def _mm_nt_fwd(a, b):
    return _dot(a, b, 1, 1), (a, b)


def _mm_nt_bwd(res, g):
    a, b = res
    return _dot(g, b, 1, 0), _dot(g, a, 0, 0)


_mm_nt.defvjp(_mm_nt_fwd, _mm_nt_bwd)


@jax.custom_vjp
def _saved_inverse(m, t_inv):
    del m
    return t_inv


def _saved_inverse_fwd(m, t_inv):
    del m
    return t_inv, t_inv


def _saved_inverse_bwd(t_inv, g):
    return -_dot3(_dot3(t_inv, g, 0, 0), t_inv, 1, 1), jnp.zeros_like(t_inv)


_saved_inverse.defvjp(_saved_inverse_fwd, _saved_inverse_bwd)


def _sigmoid(z):
    return 1.0 / (1.0 + jnp.exp(-z))


def _softplus(z):
    return jnp.maximum(z, 0.0) + jnp.log(1.0 + jnp.exp(-jnp.abs(z)))


def _silu(z):
    return z * _sigmoid(z)


def _rms_scale(x):
    return lax.rsqrt(jnp.mean(x * x, axis=-1, keepdims=True) + EPS)


def _rms_bwd(x, w, g):
    r = _rms_scale(x)
    gw = g * w
    dx = r * gw - x * (r * r * r) * jnp.mean(gw * x, axis=-1, keepdims=True)
    return dx, g * x * r


def _matmul(a, b, *, name, ta=False, tb=False, tm=512, tn=512, tk=512, out_dtypes=(F32,), b3=False, o3=False,
            extra=(), epilogue=None, exchange=None, n_sums=0):
    m, k = (a.shape[1], a.shape[0]) if ta else a.shape
    if b3:
        n = b.shape[1] if tb else b.shape[0] * b.shape[2]
        kb = b.shape[0] * b.shape[2] if tb else b.shape[1]
    else:
        n, kb = (b.shape[0], b.shape[1]) if tb else (b.shape[1], b.shape[0])
    assert kb == k, (name, kb, k)
    tm, tn, tk = min(tm, m), min(tn, n), min(tk, k)
    assert m % tm == 0 and n % tn == 0 and k % tk == 0, (name, m, n, k, tm, tn, tk)
    nk = k // tk
    whole_k_blocks = b3 and tb and not ta and nk == 1 and b.shape[0] > 1
    n_extra = len(extra)
    n_out = len(out_dtypes)
    grid = (m // tm, n // tn, nk)
    ex_in, ex_in_specs, ex_out_specs, ex_out_shape, ex_scratch = _hosted(exchange)

    def body(*refs):
        a_ref, b_ref = refs[0], refs[1]
        extra_refs = refs[2:2 + n_extra]
        first_out = 2 + n_extra + len(ex_in)
        out_refs = refs[first_out:first_out + n_out]
        ex_refs = refs[2 + n_extra:first_out] + refs[first_out + n_out:first_out + n_out + len(ex_out_shape)] + refs[-2:]
        step = [pl.program_id(d) for d in range(3)]

        if exchange is not None:
            @pl.when((step[0] == 0) & (step[1] == 0) & (step[2] == 0))
            def _():
                exchange.start(*exchange.split(ex_refs))

        def finish(acc):
            outs = (acc,) if epilogue is None else epilogue(acc, *[r[...] for r in extra_refs])
            for o_ref, val in zip(out_refs[:n_out - n_sums], outs):
                o_ref[...] = val.astype(o_ref.dtype)
            for o_ref, val in zip(out_refs[n_out - n_sums:], outs[n_out - n_sums:]):
                @pl.when(step[0] == 0)
                def _(o_ref=o_ref, val=val):
                    o_ref[...] = val

                @pl.when(step[0] > 0)
                def _(o_ref=o_ref, val=val):
                    o_ref[...] += val

        if whole_k_blocks:
            width = b.shape[2]
            part = _dot(a_ref[:, 0:width], b_ref[0], 1, 1)
            for blk in range(1, b.shape[0]):
                part = part + _dot(a_ref[:, blk * width:(blk + 1) * width], b_ref[blk], 1, 1)
        else:
            part = _dot(a_ref[...], b_ref[...], 0 if ta else 1, 1 if tb else 0)
        if nk == 1:
            finish(part)
        else:
            acc_ref = refs[first_out + n_out + len(ex_out_shape)]

            @pl.when(step[2] == 0)
            def _():
                acc_ref[...] = part

            @pl.when(step[2] > 0)
            def _():
                acc_ref[...] += part

            @pl.when(step[2] == nk - 1)
            def _():
                finish(acc_ref[...])

        if exchange is not None:
            flat = (step[0] * grid[1] + step[1]) * nk + step[2]
            total = grid[0] * grid[1] * nk

            @pl.when(flat == total // 2)
            def _():
                exchange.middle(*exchange.split(ex_refs))

            @pl.when(flat == total - 1)
            def _():
                exchange.rest(*exchange.split(ex_refs))

    a_spec = pl.BlockSpec((tk, tm), lambda i, j, kk: (kk, i)) if ta else pl.BlockSpec((tm, tk), lambda i, j, kk: (i, kk))
    if whole_k_blocks:
        b_spec = pl.BlockSpec((b.shape[0], tn, b.shape[2]), lambda i, j, kk: (0, j, 0))
    elif b3 and tb:
        assert b.shape[2] == tk
        b_spec = pl.BlockSpec((None, tn, tk), lambda i, j, kk: (kk, j, 0))
    elif b3:
        assert b.shape[2] == tn
        b_spec = pl.BlockSpec((None, tk, tn), lambda i, j, kk: (j, kk, 0))
    elif tb:
        b_spec = pl.BlockSpec((tn, tk), lambda i, j, kk: (j, kk))
    else:
        b_spec = pl.BlockSpec((tk, tn), lambda i, j, kk: (kk, j))
    tile = pl.BlockSpec((tm, tn), lambda i, j, kk: (i, j))
    out_specs = [tile] * n_out
    out_shape = [jax.ShapeDtypeStruct((m, n), dt) for dt in out_dtypes]
    if o3:
        out_specs[0] = pl.BlockSpec((None, tm, tn), lambda i, j, kk: (j, i, 0))
        out_shape[0] = jax.ShapeDtypeStruct((n // tn, m, tn), out_dtypes[0])
    assert n_sums == 0 or tn == n
    for r in range(n_out - n_sums, n_out):
        out_specs[r] = pl.BlockSpec((1, tn), lambda i, j, kk: (0, 0))
        out_shape[r] = jax.ShapeDtypeStruct((1, n), out_dtypes[r])
    res = pl.pallas_call(
        body, name=name, grid=grid,
        in_specs=[a_spec, b_spec] + [tile if e.shape[0] == m else pl.BlockSpec((1, tn), lambda i, j, kk: (0, j)) for e in extra]
        + ex_in_specs, out_specs=out_specs + ex_out_specs,
        out_shape=out_shape + ex_out_shape,
        scratch_shapes=([pltpu.VMEM((tm, tn), F32)] if nk > 1 else []) + ex_scratch,
        compiler_params=_cparams(),
    )(a, b, *extra, *ex_in)
    if exchange is not None:
        return (res[0] if n_out == 1 else res[:n_out]), res[n_out:]
    return res[0] if n_out == 1 else res


TR = 512


def _row_spec(cols):
    return pl.BlockSpec((TR, cols), lambda i: (i, 0))


def _vec_spec(cols):
    return pl.BlockSpec((1, cols), lambda i: (0, 0))


def _pre_norm(x, w, exchange=None):
    ex_in, ex_in_specs, ex_out_specs, ex_out_shape, ex_scratch = _hosted(exchange)

    def body(*refs):
        x_ref, w_ref, h_ref = refs[0], refs[1], refs[2 + len(ex_in)]
        ex_refs = refs[2:2 + len(ex_in)] + refs[3 + len(ex_in):]
        if exchange is not None:
            @pl.when(pl.program_id(0) == 0)
            def _():
                exchange.start(*exchange.split(ex_refs))

        xv = x_ref[...]
        h_ref[...] = (xv * _rms_scale(xv) * w_ref[...]).astype(BF16)

        if exchange is not None:
            @pl.when(pl.program_id(0) == S // TR - 1)
            def _():
                exchange.finish(*exchange.split(ex_refs))

    res = pl.pallas_call(
        body, name="pre_norm", grid=(S // TR,), in_specs=[_row_spec(D), _vec_spec(D)] + ex_in_specs,
        out_specs=[_row_spec(D)] + ex_out_specs, out_shape=[jax.ShapeDtypeStruct((S, D), BF16)] + ex_out_shape,
        scratch_shapes=ex_scratch, compiler_params=_cparams(),
    )(x, w, *ex_in)
    return res[0], res[1:]


def _pre_norm_bwd(dh, x, w, dx1):
    def body(dh_ref, x_ref, w_ref, dx1_ref, dx_ref, dw_ref):
        i = pl.program_id(0)
        dxa, dwt = _rms_bwd(x_ref[...], w_ref[...], dh_ref[...])
        dx_ref[...] = dx1_ref[...] + dxa

        @pl.when(i == 0)
        def _():
            dw_ref[...] = jnp.zeros_like(dw_ref)

        dw_ref[...] += jnp.sum(dwt, axis=0, keepdims=True)

    return pl.pallas_call(
        body, name="pre_norm_bwd", grid=(S // TR,),
        in_specs=[_row_spec(D), _row_spec(D), _vec_spec(D), _row_spec(D)], out_specs=[_row_spec(D), _vec_spec(D)],
        out_shape=[jax.ShapeDtypeStruct((S, D), F32), jax.ShapeDtypeStruct((1, D), F32)], compiler_params=_cparams(),
    )(dh, x, w, dx1)


BQ = 512
NQ = S // BQ
LANE_BETA, LANE_G = 8, 12


def _gate_lanes(shape):
    lane = lax.broadcasted_iota(jnp.int32, shape, 1)
    return lane < LANE_BETA, (lane >= LANE_BETA) & (lane < LANE_G), (lane >= LANE_G) & (lane < LANE_G + NGH)


def _gates(proj, bias_vec, alog_vec):
    def body(s_ref, b_ref, a_ref, o_ref, carry_ref):
        i = pl.program_id(0)

        @pl.when(i == 0)
        def _():
            carry_ref[...] = jnp.zeros_like(carry_ref)

        z = s_ref[...] + b_ref[...]
        tail = jnp.log(1.0 + jnp.exp(-jnp.abs(z)))
        sp = jnp.maximum(z, 0.0) + tail
        lf = jnp.minimum(z, 0.0) - tail
        r = lax.broadcasted_iota(jnp.int32, (BQ, BQ), 0)
        c = lax.broadcasted_iota(jnp.int32, (BQ, BQ), 1)
        tri = (c <= r).astype(F32)
        cum = _hdot(tri, lf) + carry_ref[...]
        carry_ref[...] = cum[BQ - 1:BQ, :]
        is_fox, is_beta, is_g = _gate_lanes(z.shape)
        o_ref[...] = jnp.where(is_fox, cum, jnp.where(is_beta, _sigmoid(z), jnp.where(is_g, -jnp.exp(a_ref[...]) * sp, 0.0)))

    return pl.pallas_call(
        body, name="gates", grid=(NQ,),
        in_specs=[pl.BlockSpec((BQ, LANES), lambda i: (i, BLK_SMALL)), _vec_spec(LANES), _vec_spec(LANES)],
        out_specs=pl.BlockSpec((BQ, LANES), lambda i: (i, 0)), out_shape=jax.ShapeDtypeStruct((S, LANES), F32),
        scratch_shapes=[pltpu.VMEM((1, LANES), F32)], compiler_params=_cparams(),
    )(proj, bias_vec, alog_vec)


def _gates_bwd(proj, bias_vec, alog_vec, dgates_gdn, dcum_fox, dproj):
    def body(s_ref, b_ref, a_ref, dg_ref, dc_ref, dproj_in, dproj_ref, red_ref, carry_ref):
        del dproj_in
        i = pl.program_id(0)

        @pl.when(i == 0)
        def _():
            carry_ref[...] = jnp.zeros_like(carry_ref)
            red_ref[...] = jnp.zeros_like(red_ref)

        z = s_ref[...] + b_ref[...]
        dg = dg_ref[...] + dc_ref[...]
        r = lax.broadcasted_iota(jnp.int32, (BQ, BQ), 0)
        c = lax.broadcasted_iota(jnp.int32, (BQ, BQ), 1)
        upper = (c >= r).astype(F32)
        dlf = _hdot(upper, dg) + carry_ref[...]
        carry_ref[...] = dlf[0:1, :]
        sig = _sigmoid(z)
        g_scale = -jnp.exp(a_ref[...])
        is_fox, is_beta, is_g = _gate_lanes(z.shape)
        ds = jnp.where(is_fox, dlf * (1.0 - sig), jnp.where(is_beta, dg * sig * (1.0 - sig), jnp.where(is_g, dg * g_scale * sig, 0.0)))
        dproj_ref[:, 0:LANES] = ds.astype(BF16)
        dproj_ref[:, LANES:2 * LANES] = jnp.zeros((BQ, LANES), BF16)
        dalog = jnp.where(is_g, dg * g_scale * _softplus(z), 0.0)
        sums = jnp.sum(ds, axis=0, keepdims=True)
        red_ref[0:1, :] += jnp.where(is_fox[0:1], sums, 0.0)
        red_ref[1:2, :] += pltpu.roll(jnp.where(is_g[0:1], sums, 0.0), LANES - LANE_G, 1)
        red_ref[2:3, :] += pltpu.roll(jnp.sum(dalog, axis=0, keepdims=True), LANES - LANE_G, 1)

    blk = pl.BlockSpec((BQ, LANES), lambda i: (NQ - 1 - i, 0))
    return pl.pallas_call(
        body, name="gates_bwd", grid=(NQ,),
        in_specs=[pl.BlockSpec((BQ, LANES), lambda i: (NQ - 1 - i, BLK_SMALL)), _vec_spec(LANES), _vec_spec(LANES), blk, blk,
                  pl.BlockSpec(memory_space=pl.ANY)],
        out_specs=[pl.BlockSpec((BQ, 2 * LANES), lambda i: (NQ - 1 - i, BLK_SMALL // 2)), pl.BlockSpec((8, LANES), lambda i: (0, 0))],
        out_shape=[jax.ShapeDtypeStruct((S, DPROJ_PAD), BF16), jax.ShapeDtypeStruct((8, LANES), F32)],
        input_output_aliases={5: 0},
        scratch_shapes=[pltpu.VMEM((1, LANES), F32)], compiler_params=_cparams(),
    )(proj, bias_vec, alog_vec, dgates_gdn, dcum_fox, dproj)


FOX_SCALE = FHD ** -0.5
FOX_PAIRS = 2
FOX_PAIRS_BWD = 2


def _head_mask(e):
    lane = lax.broadcasted_iota(jnp.int32, (1, LANES), 1)
    return (lane >= e * FHD) & (lane < (e + 1) * FHD)


def _lane_col(vals, index):
    lane = lax.broadcasted_iota(jnp.int32, vals.shape, 1)
    return jnp.sum(jnp.where(lane == index, vals, 0.0), axis=1, keepdims=True)


def _sublane_row(vals, index):
    row = lax.broadcasted_iota(jnp.int32, vals.shape, 0)
    return jnp.sum(jnp.where(row == index, vals, 0.0), axis=0, keepdims=True)


def _pair_cols(c0, c1):
    lane = lax.broadcasted_iota(jnp.int32, (c0.shape[0], 2), 1)
    return jnp.where(lane == 0, c0, c1)


def _split3(x):
    hi = x.astype(BF16).astype(F32)
    rest = x - hi
    mid = rest.astype(BF16).astype(F32)
    return hi, mid, (rest - mid).astype(BF16).astype(F32)


def _fox_operand(vals, e, cum, is_query):
    lane = lax.broadcasted_iota(jnp.int32, (1, LANES), 1)
    base = (1 - e) * FHD
    parts = _split3(cum)
    own = jnp.where(_head_mask(e), vals * FOX_SCALE if is_query else vals, 0.0)
    cum_at, ones_at = (base, base + 3) if is_query else (base + 3, base)
    sign = 1.0 if is_query else -1.0
    out = own + jnp.where((lane >= ones_at) & (lane < ones_at + 3), 1.0, 0.0)
    for i, part in enumerate(parts):
        out = out + jnp.where(lane == cum_at + i, sign * part, 0.0)
    return out.astype(BF16)


def _causal_block():
    return lax.broadcasted_iota(jnp.int32, (BQ, BQ), 1) <= lax.broadcasted_iota(jnp.int32, (BQ, BQ), 0)


def _head_rms(o, masks):
    o2 = o * o
    r = [lax.rsqrt(jnp.sum(jnp.where(mk, o2, 0.0), axis=1, keepdims=True) * (1.0 / FHD) + EPS) for mk in masks]
    return jnp.where(masks[0], r[0], r[1])


def _hosted(exchange):
    if exchange is None:
        return [], [], [], [], []
    return (exchange.inputs, [HBM] * len(exchange.inputs), [HBM] * len(exchange.out_shape), exchange.out_shape,
            exchange.sem_shapes())


def _fox_fwd(proj, gates, w2, exchange=None):
    ex_in, ex_in_specs, ex_out_specs, ex_out_shape, ex_scratch = _hosted(exchange)

    n_in = 3 * FOX_PAIRS + 2
    heads = [(pp, e) for pp in range(FOX_PAIRS) for e in range(2)]

    def body(*refs):
        qkv_refs, g_ref, w_ref = refs[:3 * FOX_PAIRS], refs[3 * FOX_PAIRS], refs[3 * FOX_PAIRS + 1]
        mix_ref, o_ref, lse_ref = refs[n_in + len(ex_in):n_in + 3 + len(ex_in)]
        ka_ref, vb_ref = refs[n_in + 3 + len(ex_in) + len(ex_out_shape):n_in + 5 + len(ex_in) + len(ex_out_shape)]
        ex_refs = refs[n_in:n_in + len(ex_in)] + refs[n_in + 3 + len(ex_in):n_in + 3 + len(ex_in) + len(ex_out_shape)] + refs[-2:]
        grp, qi = pl.program_id(0), pl.program_id(1)

        def head_index(pp, e):
            return 2 * (FOX_PAIRS * grp + pp) + e

        if exchange is not None:
            @pl.when((grp == 0) & (qi == 0))
            def _():
                exchange.start(*exchange.split(ex_refs))

        @pl.when(qi == 0)
        def _():
            gt = g_ref[...]
            for pp in range(FOX_PAIRS):
                kv = qkv_refs[3 * pp + 1][...]
                for e in range(2):
                    ka_ref[2 * pp + e] = _fox_operand(kv, e, _lane_col(gt, head_index(pp, e)), False)
                vb_ref[pp] = qkv_refs[3 * pp + 2][...].astype(BF16)

        masks = [_head_mask(0), _head_mask(1)]
        gt = g_ref[pl.ds(pl.multiple_of(qi * BQ, BQ), BQ), :]
        qs = [_fox_operand(qkv_refs[3 * pp][...], e, _lane_col(gt, head_index(pp, e)), True) for pp, e in heads]
        n = range(len(heads))

        def block(kj, carry, diagonal):
            rows = pl.ds(pl.multiple_of(kj * BQ, BQ), BQ)
            s = [_dot(qs[i], ka_ref[i, rows, :], 1, 1) for i in n]
            if diagonal:
                s = [jnp.where(_causal_block(), s[i], -jnp.inf) for i in n]
            m_new = [jnp.maximum(carry[i][0], jnp.max(s[i], axis=-1, keepdims=True)) for i in n]
            p = [jnp.exp(s[i] - m_new[i]) for i in n]
            alpha = [jnp.exp(carry[i][0] - m_new[i]) for i in n]
            l_new = [alpha[i] * carry[i][1] + jnp.sum(p[i], axis=-1, keepdims=True) for i in n]
            pv = [_dot(p[i], vb_ref[heads[i][0], rows, :]) for i in n]
            return tuple((m_new[i], l_new[i], alpha[i] * carry[i][2] + pv[i]) for i in n)

        one = (jnp.full((BQ, 1), -jnp.inf, F32), jnp.zeros((BQ, 1), F32), jnp.zeros((BQ, LANES), F32))
        below = lax.fori_loop(0, qi, lambda kj, carry: block(kj, carry, False), (one,) * len(heads))
        done = block(qi, below, True)
        for pp in range(FOX_PAIRS):
            (m0, l0, a0), (m1, l1, a1) = done[2 * pp], done[2 * pp + 1]
            o = jnp.where(masks[0], a0 / l0, a1 / l1)
            cols = slice(pp * LANES, (pp + 1) * LANES)
            o_ref[:, cols] = o
            mix_ref[:, cols] = (o * _head_rms(o, masks) * w_ref[...]).astype(BF16)
            lse_ref[pp] = _pair_cols(m0 + jnp.log(l0), m1 + jnp.log(l1))

        if exchange is not None:
            @pl.when((grp == NPAIR // FOX_PAIRS // 2) & (qi == 0))
            def _():
                exchange.middle(*exchange.split(ex_refs))

            @pl.when((grp == NPAIR // FOX_PAIRS - 1) & (qi == NQ - 1))
            def _():
                exchange.rest(*exchange.split(ex_refs))

    qkv_specs = []
    for pp in range(FOX_PAIRS):
        qkv_specs.append(pl.BlockSpec((BQ, LANES), lambda g, i, pp=pp: (i, 3 * (FOX_PAIRS * g + pp))))
        qkv_specs.append(pl.BlockSpec((S, LANES), lambda g, i, pp=pp: (0, 3 * (FOX_PAIRS * g + pp) + 1)))
        qkv_specs.append(pl.BlockSpec((S, LANES), lambda g, i, pp=pp: (0, 3 * (FOX_PAIRS * g + pp) + 2)))
    blk = pl.BlockSpec((BQ, FOX_PAIRS * LANES), lambda g, i: (i, g))
    res = pl.pallas_call(
        body, name="fox_fwd", grid=(NPAIR // FOX_PAIRS, NQ),
        in_specs=qkv_specs + [pl.BlockSpec((S, LANES), lambda g, i: (0, 0)), pl.BlockSpec((1, LANES), lambda g, i: (0, 0))]
        + ex_in_specs,
        out_specs=[blk, blk, pl.BlockSpec((FOX_PAIRS, BQ, 2), lambda g, i: (g, i, 0))] + ex_out_specs,
        out_shape=[jax.ShapeDtypeStruct((S, D), BF16), jax.ShapeDtypeStruct((S, DFOX), F32),
                   jax.ShapeDtypeStruct((NPAIR, S, 2), F32)] + ex_out_shape,
        scratch_shapes=[pltpu.VMEM((2 * FOX_PAIRS, S, LANES), BF16), pltpu.VMEM((FOX_PAIRS, S, LANES), BF16)] + ex_scratch,
        compiler_params=_cparams(),
    )(*([proj] * (3 * FOX_PAIRS)), gates, w2, *ex_in)
    return res[0], res[1], res[2], res[3:]


def _fox_norm_bwd(o, dmix, w2, exchange=None):
    ex_in, ex_in_specs, ex_out_specs, ex_out_shape, ex_scratch = _hosted(exchange)

    def body(*refs):
        o_ref, g_ref, w_ref = refs[:3]
        do_ref, dl_ref, dw_ref = refs[3 + len(ex_in):6 + len(ex_in)]
        ex_refs = refs[3:3 + len(ex_in)] + refs[6 + len(ex_in):]
        hp, qi = pl.program_id(0), pl.program_id(1)

        if exchange is not None:
            @pl.when((hp == 0) & (qi == 0))
            def _():
                exchange.start(*exchange.split(ex_refs))

        masks = [_head_mask(0), _head_mask(1)]
        ov = o_ref[...]
        g = g_ref[...]
        r = _head_rms(ov, masks)
        gw = g * w_ref[...]
        gwo = gw * ov
        mean = [jnp.sum(jnp.where(mk, gwo, 0.0), axis=1, keepdims=True) * (1.0 / FHD) for mk in masks]
        do = r * gw - ov * (r * r * r) * jnp.where(masks[0], mean[0], mean[1])
        do_ref[...] = do.astype(BF16)
        doo = do * ov
        dl_ref[...] = _pair_cols(*[jnp.sum(jnp.where(mk, doo, 0.0), axis=1, keepdims=True) for mk in masks])

        @pl.when((hp == 0) & (qi == 0))
        def _():
            dw_ref[...] = jnp.zeros_like(dw_ref)

        dw_ref[...] += jnp.sum(g * ov * r, axis=0, keepdims=True)

        @pl.when((hp == NPAIR - 1) & (qi == NQ - 1))
        def _():
            dw = dw_ref[...]
            dw_ref[...] = dw + pltpu.roll(dw, FHD, 1)
            if exchange is not None:
                exchange.finish(*exchange.split(ex_refs))

    blk = pl.BlockSpec((BQ, LANES), lambda hp, i: (i, hp))
    vec = pl.BlockSpec((1, LANES), lambda hp, i: (0, 0))
    res = pl.pallas_call(
        body, name="fox_norm_bwd", grid=(NPAIR, NQ), in_specs=[blk, blk, vec] + ex_in_specs,
        out_specs=[blk, pl.BlockSpec((None, BQ, 2), lambda hp, i: (hp, i, 0)), vec] + ex_out_specs,
        out_shape=[jax.ShapeDtypeStruct((S, DFOX), BF16), jax.ShapeDtypeStruct((NPAIR, S, 2), F32),
                   jax.ShapeDtypeStruct((1, LANES), F32)] + ex_out_shape,
        scratch_shapes=ex_scratch, compiler_params=_cparams(),
    )(o, dmix, w2, *ex_in)
    return res[0], res[1], res[2], res[3:]


def _fox_bwd(proj, do, gates, lse, delta, exchange=None):
    ex_in, ex_in_specs, ex_out_specs, ex_out_shape, ex_scratch = _hosted(exchange)

    pg = FOX_PAIRS_BWD
    n_in = 3 * pg + 4
    heads = [(pp, e) for pp in range(pg) for e in range(2)]

    def body(*refs):
        qkv_refs = refs[:3 * pg]
        do_ref, g_ref, lse_ref, dl_ref = refs[3 * pg:n_in]
        dproj_ref, dc_ref = refs[n_in + len(ex_in):n_in + 2 + len(ex_in)]
        qa_ref, dq_ref = refs[n_in + 2 + len(ex_in) + len(ex_out_shape):n_in + 4 + len(ex_in) + len(ex_out_shape)]
        ex_refs = refs[n_in:n_in + len(ex_in)] + refs[n_in + 2 + len(ex_in):n_in + 2 + len(ex_in) + len(ex_out_shape)] + refs[-2:]
        grp, kj = pl.program_id(0), pl.program_id(1)

        def head_index(pp, e):
            return 2 * (pg * grp + pp) + e

        if exchange is not None:
            @pl.when((grp == 0) & (kj == 0))
            def _():
                exchange.start(*exchange.split(ex_refs))

        @pl.when(kj == 0)
        def _():
            gt = g_ref[...]
            for pp in range(pg):
                qv = qkv_refs[3 * pp][...]
                for e in range(2):
                    qa_ref[2 * pp + e] = _fox_operand(qv, e, _lane_col(gt, head_index(pp, e)), True)
            dq_ref[...] = jnp.zeros_like(dq_ref)

        @pl.when((grp == 0) & (kj == 0))
        def _():
            dc_ref[...] = jnp.zeros_like(dc_ref)

        masks = [_head_mask(0), _head_mask(1)]
        krows = pl.ds(pl.multiple_of(kj * BQ, BQ), BQ)
        gk = g_ref[krows, :]
        kas = [_fox_operand(qkv_refs[3 * pp + 1][...], e, _lane_col(gk, head_index(pp, e)), False) for pp, e in heads]
        vbs = [qkv_refs[3 * pp + 2][...].astype(BF16) for pp in range(pg)]
        lane = lax.broadcasted_iota(jnp.int32, (BQ, LANES), 1)
        n = range(len(heads))

        def block(qi, carry, diagonal):
            dks, dvs, css = carry
            rows = pl.ds(pl.multiple_of(qi * BQ, BQ), BQ)
            qa = [qa_ref[i, rows, :] for i in n]
            s = [_dot(qa[i], kas[i], 1, 1) for i in n]
            if diagonal:
                s = [jnp.where(_causal_block(), s[i], -jnp.inf) for i in n]
            dov = [do_ref[rows, pp * LANES:(pp + 1) * LANES] for pp in range(pg)]
            doe = [jnp.where(masks[e], dov[pp], jnp.zeros_like(dov[pp])) for pp, e in heads]
            lse2 = [lse_ref[pp, rows, :] for pp in range(pg)]
            dl2 = [dl_ref[pp, rows, :] for pp in range(pg)]
            p = [jnp.exp(s[i] - _lane_col(lse2[heads[i][0]], heads[i][1])) for i in n]
            dp = [_dot(doe[i], vbs[heads[i][0]], 1, 1) for i in n]
            ds = [p[i] * (dp[i] - _lane_col(dl2[heads[i][0]], heads[i][1])) for i in n]
            dv_part = [_dot(p[i], doe[i], 0, 0) for i in n]
            dk_part = [_dot(ds[i], jnp.where(masks[heads[i][1]], qa[i], jnp.zeros_like(qa[i])), 0, 0) for i in n]
            dq_part = [jnp.where(masks[heads[i][1]], _dot(ds[i], kas[i]), 0.0) for i in n]
            css = tuple(css[i] + jnp.sum(ds[i], axis=0, keepdims=True) for i in n)
            dc = jnp.zeros((BQ, LANES), F32)
            for i in n:
                dc = dc + jnp.where(lane == head_index(*heads[i]), jnp.sum(ds[i], axis=1, keepdims=True), 0.0)
            for pp in range(pg):
                dq_ref[pp, rows, :] += (dq_part[2 * pp] + dq_part[2 * pp + 1]) * FOX_SCALE
            dc_ref[rows, :] += dc
            dks = tuple(dks[pp] + dk_part[2 * pp] + dk_part[2 * pp + 1] for pp in range(pg))
            dvs = tuple(dvs[pp] + dv_part[2 * pp] + dv_part[2 * pp + 1] for pp in range(pg))
            return dks, dvs, css

        zero = jnp.zeros((BQ, LANES), F32)
        first = block(kj, ((zero,) * pg, (zero,) * pg, (jnp.zeros((1, BQ), F32),) * len(heads)), True)
        dks, dvs, css = lax.fori_loop(kj + 1, NQ, lambda qi, carry: block(qi, carry, False), first)
        r = lax.broadcasted_iota(jnp.int32, (BQ, BQ), 0)
        c = lax.broadcasted_iota(jnp.int32, (BQ, BQ), 1)
        dcol = jnp.zeros((BQ, LANES), F32)
        for i in n:
            col = jnp.sum(jnp.where(r == c, css[i], 0.0), axis=1, keepdims=True)
            dcol = dcol + jnp.where(lane == head_index(*heads[i]), col, 0.0)
        dc_ref[krows, :] -= dcol
        for pp in range(pg):
            base = 3 * pp * LANES
            dproj_ref[krows, base + LANES:base + 2 * LANES] = dks[pp].astype(BF16)
            dproj_ref[krows, base + 2 * LANES:base + 3 * LANES] = dvs[pp].astype(BF16)

        @pl.when(kj == NQ - 1)
        def _():
            for pp in range(pg):
                dproj_ref[:, 3 * pp * LANES:(3 * pp + 1) * LANES] = dq_ref[pp].astype(BF16)

        if exchange is not None:
            @pl.when((grp == NPAIR // pg // 2) & (kj == 0))
            def _():
                exchange.middle(*exchange.split(ex_refs))

            @pl.when((grp == NPAIR // pg - 1) & (kj == NQ - 1))
            def _():
                exchange.rest(*exchange.split(ex_refs))

    qkv_specs = []
    for pp in range(pg):
        qkv_specs.append(pl.BlockSpec((S, LANES), lambda g, j, pp=pp: (0, 3 * (pg * g + pp))))
        qkv_specs.append(pl.BlockSpec((BQ, LANES), lambda g, j, pp=pp: (j, 3 * (pg * g + pp) + 1)))
        qkv_specs.append(pl.BlockSpec((BQ, LANES), lambda g, j, pp=pp: (j, 3 * (pg * g + pp) + 2)))
    pair = pl.BlockSpec((pg, S, 2), lambda g, j: (g, 0, 0))
    res = pl.pallas_call(
        body, name="fox_bwd", grid=(NPAIR // pg, NQ),
        in_specs=qkv_specs + [pl.BlockSpec((S, pg * LANES), lambda g, j: (0, g)), pl.BlockSpec((S, LANES), lambda g, j: (0, 0)),
                              pair, pair] + ex_in_specs,
        out_specs=[pl.BlockSpec((S, 3 * pg * LANES), lambda g, j: (0, g)), pl.BlockSpec((S, LANES), lambda g, j: (0, 0))]
        + ex_out_specs,
        out_shape=[jax.ShapeDtypeStruct((S, DPROJ_PAD), BF16), jax.ShapeDtypeStruct((S, LANES), F32)] + ex_out_shape,
        scratch_shapes=[pltpu.VMEM((2 * pg, S, LANES), BF16), pltpu.VMEM((pg, S, LANES), F32)] + ex_scratch,
        compiler_params=_cparams(),
    )(*([proj] * (3 * pg)), do, gates, lse, delta, *ex_in)
    return res[0], res[1], res[2:]


NQKV = 3 * NGH
GDN_QSCALE = GHD ** -0.5


def _shift_down(x, s):
    if s == 0:
        return x
    row = lax.broadcasted_iota(jnp.int32, x.shape, 0)
    return jnp.where(row >= s, pltpu.roll(x, s, 0), 0.0)


def _shift_up(x, s):
    if s == 0:
        return x
    n = x.shape[0]
    row = lax.broadcasted_iota(jnp.int32, x.shape, 0)
    return jnp.where(row < n - s, pltpu.roll(x, n - s, 0), 0.0)


def _conv_taps(xv):
    return [_shift_down(xv, CONV_K - 1 - j) for j in range(CONV_K)]


def _conv_pre(taps, wv):
    pre = taps[CONV_K - 1] * wv[CONV_K - 1:CONV_K, :]
    for j in range(CONV_K - 1):
        pre = pre + taps[j] * wv[j:j + 1, :]
    return pre


def _l2_factors(b):
    return b < 2 * NGH, jnp.where(b < NGH, GDN_QSCALE, 1.0)


def _gdn_pre(proj, conv_w):
    def body(x_ref, w_ref, o_ref):
        b = pl.program_id(0)
        c = _silu(_conv_pre(_conv_taps(x_ref[...]), w_ref[...]))
        normed, scale = _l2_factors(b)
        rs = lax.rsqrt(jnp.sum(c * c, axis=-1, keepdims=True) + EPS)
        o_ref[...] = c * jnp.where(normed, rs, 1.0) * scale

    return pl.pallas_call(
        body, name="gdn_pre", grid=(NQKV,),
        in_specs=[pl.BlockSpec((S, GHD), lambda b: (0, BLK_GDN + b)), pl.BlockSpec((CONV_K, GHD), lambda b: (0, b))],
        out_specs=pl.BlockSpec((S, GHD), lambda b: (0, b)),
        out_shape=jax.ShapeDtypeStruct((S, NQKV * GHD), F32), compiler_params=_cparams(),
    )(proj, conv_w)


def _gdn_pre_bwd(proj, conv_w, dqkv, dproj):
    def body(x_ref, w_ref, dy_ref, dproj_in, dx_ref, dw_ref):
        del dproj_in
        b = pl.program_id(0)
        taps = _conv_taps(x_ref[...])
        wv = w_ref[...]
        pre = _conv_pre(taps, wv)
        sig = _sigmoid(pre)
        c = pre * sig
        normed, scale = _l2_factors(b)
        g = dy_ref[...] * scale
        rs = lax.rsqrt(jnp.sum(c * c, axis=-1, keepdims=True) + EPS)
        dc_n = rs * g - c * (rs * rs * rs) * jnp.sum(g * c, axis=-1, keepdims=True)
        dc = jnp.where(normed, dc_n, g)
        dpre = dc * sig * (1.0 + pre * (1.0 - sig))
        dx = dpre * wv[CONV_K - 1:CONV_K, :]
        for j in range(CONV_K - 1):
            dx = dx + _shift_up(dpre, CONV_K - 1 - j) * wv[j:j + 1, :]
        dx_ref[...] = dx.astype(BF16)
        for j in range(CONV_K):
            dw_ref[j:j + 1, :] = jnp.sum(dpre * taps[j], axis=0, keepdims=True)

    return pl.pallas_call(
        body, name="gdn_pre_bwd", grid=(NQKV,),
        in_specs=[pl.BlockSpec((S, GHD), lambda b: (0, BLK_GDN + b)), pl.BlockSpec((CONV_K, GHD), lambda b: (0, b)),
                  pl.BlockSpec((None, S, GHD), lambda b: (b // NGH, 0, b % NGH)), pl.BlockSpec(memory_space=pl.ANY)],
        out_specs=[pl.BlockSpec((S, GHD), lambda b: (0, BLK_GDN + b)), pl.BlockSpec((CONV_K, GHD), lambda b: (0, b))],
        out_shape=[jax.ShapeDtypeStruct((S, DPROJ_PAD), BF16), jax.ShapeDtypeStruct((CONV_K, NQKV * GHD), F32)],
        input_output_aliases={3: 0}, compiler_params=_cparams(),
    )(proj, conv_w, dqkv, dproj)


CB = 16
NCB = NCH // CB


def _chunk_prep(qs, ks, vs, gcols, bcols, t_saved=None):
    n = range(len(qs))
    r = lax.broadcasted_iota(jnp.int32, (CHUNK, CHUNK), 0)
    c = lax.broadcasted_iota(jnp.int32, (CHUNK, CHUNK), 1)
    incl = c <= r
    eye = (r == c).astype(F32)
    grow = [jnp.sum(gcols[i] * eye, axis=0, keepdims=True) for i in n]
    gc_col = [jnp.sum(jnp.where(incl, grow[i], 0.0), axis=1, keepdims=True) for i in n]
    gc_row = [jnp.sum(jnp.where(r <= c, gcols[i], 0.0), axis=0, keepdims=True) for i in n]
    decay = [jnp.exp(jnp.where(incl, gc_col[i] - gc_row[i], -jnp.inf)) for i in n]
    kb = [ks[i] * bcols[i] for i in n]
    vb = [vs[i] * bcols[i] for i in n]
    kk = [_mm_nt(kb[i], ks[i]) for i in n]
    m = [jnp.where(c < r, kk[i] * decay[i], 0.0) for i in n]
    if t_saved is None:
        t_inv = [eye - m[i] for i in n]
        p = [_dot3(m[i], m[i]) for i in n]
        for step in range(5):
            t_inv = [t_inv[i] + _dot3(t_inv[i], p[i]) for i in n]
            if step < 4:
                p = [_dot3(p[i], p[i]) for i in n]
    else:
        t_inv = [_saved_inverse(m[i], t_saved[i]) for i in n]
    egc = [jnp.exp(gc_col[i]) for i in n]
    u = [_mm_nn(t_inv[i], vb[i]) for i in n]
    w = [_mm_nn(t_inv[i], kb[i] * egc[i]) for i in n]
    qk = [_mm_nt(qs[i], ks[i]) for i in n]
    gc_last = [gc_col[i][CHUNK - 1:CHUNK, :] for i in n]
    return [(u[i], w[i], qk[i] * decay[i], qs[i] * egc[i], ks[i] * jnp.exp(gc_last[i] - gc_col[i]), jnp.exp(gc_last[i]),
             t_inv[i]) for i in n]


def _prep_specs():
    rows = CB * CHUNK
    qs = pl.BlockSpec((rows, GHD), lambda i, h: (i, h))
    ks = pl.BlockSpec((rows, GHD), lambda i, h: (i, NGH + h))
    vs = pl.BlockSpec((rows, GHD), lambda i, h: (i, 2 * NGH + h))
    gs = pl.BlockSpec((rows, LANES), lambda i, h: (i, 0))
    a_s = pl.BlockSpec((None, rows, CHUNK), lambda i, h: (h, i, 0))
    gl_s = pl.BlockSpec((None, CB, 1, LANES), lambda i, h: (h, i, 0, 0))
    return qs, ks, vs, gs, a_s, gl_s


def _gdn_prep(qkv, gates, exchange=None):
    ex_in, ex_in_specs, ex_out_specs, ex_out_shape, ex_scratch = _hosted(exchange)

    def body(*refs):
        q_ref, k_ref, v_ref, g_ref = refs[:4]
        u_ref, w_ref, qd_ref, kd_ref, a_ref, gl_ref, t_ref = refs[4 + len(ex_in):11 + len(ex_in)]
        ex_refs = refs[4:4 + len(ex_in)] + refs[11 + len(ex_in):]
        h = pl.program_id(1)

        if exchange is not None:
            @pl.when((pl.program_id(0) == 0) & (h == 0))
            def _():
                exchange.start(*exchange.split(ex_refs))

        chunks = [pl.ds(cidx * CHUNK, CHUNK) for cidx in range(CB)]
        gts = [g_ref[rows, :] for rows in chunks]
        outs = _chunk_prep([q_ref[rows, :] for rows in chunks], [k_ref[rows, :] for rows in chunks],
                           [v_ref[rows, :] for rows in chunks], [_lane_col(gt, LANE_G + h) for gt in gts],
                           [_lane_col(gt, LANE_BETA + h) for gt in gts])
        for cidx, rows in enumerate(chunks):
            u, w, a, qd, kd, gl, t_inv = outs[cidx]
            u_ref[rows, :] = u
            w_ref[rows, :] = w
            qd_ref[rows, :] = qd
            kd_ref[rows, :] = kd
            a_ref[rows, :] = a
            t_ref[rows, :] = t_inv
            gl_ref[cidx] = jnp.broadcast_to(gl, (1, LANES))

        if exchange is not None:
            @pl.when((pl.program_id(0) == NCB // 2) & (h == 0))
            def _():
                exchange.middle(*exchange.split(ex_refs))

            @pl.when((pl.program_id(0) == NCB - 1) & (h == NGH - 1))
            def _():
                exchange.rest(*exchange.split(ex_refs))

    qs, ks, vs, gs, a_s, gl_s = _prep_specs()
    tok = jax.ShapeDtypeStruct((S, DGDN), F32)
    sq = jax.ShapeDtypeStruct((NGH, S, CHUNK), F32)
    res = pl.pallas_call(
        body, name="gdn_prep", grid=(NCB, NGH), in_specs=[qs, ks, vs, gs] + ex_in_specs,
        out_specs=[qs, qs, qs, qs, a_s, gl_s, a_s] + ex_out_specs,
        out_shape=[tok, tok, tok, tok, sq, jax.ShapeDtypeStruct((NGH, NCH, 1, LANES), F32), sq] + ex_out_shape,
        scratch_shapes=ex_scratch, compiler_params=_cparams(),
    )(qkv, qkv, qkv, gates, *ex_in)
    return res[:7], res[7:]


def _gdn_prep_bwd(qkv, gates, t_inv, du, dw, dqd, dkd, da, dgl, exchange=None):
    ex_in, ex_in_specs, ex_out_specs, ex_out_shape, ex_scratch = _hosted(exchange)

    def body(*refs):
        q_ref, k_ref, v_ref, g_ref, t_ref, du_ref, dw_ref, dqd_ref, dkd_ref, da_ref, dgl_ref = refs[:11]
        dqkv_ref, dg_ref = refs[11 + len(ex_in):13 + len(ex_in)]
        ex_refs = refs[11:11 + len(ex_in)] + refs[13 + len(ex_in):]
        h = pl.program_id(1)

        if exchange is not None:
            @pl.when((pl.program_id(0) == 0) & (h == 0))
            def _():
                exchange.start(*exchange.split(ex_refs))

        @pl.when(h == 0)
        def _():
            dg_ref[...] = jnp.zeros_like(dg_ref)

        lane = lax.broadcasted_iota(jnp.int32, (CHUNK, LANES), 1)
        chunks = [pl.ds(cidx * CHUNK, CHUNK) for cidx in range(CB)]
        gts = [g_ref[rows, :] for rows in chunks]
        t_saved = [t_ref[rows, :] for rows in chunks]
        _, vjp = jax.vjp(lambda *args: [o[:6] for o in _chunk_prep(*args, t_saved=t_saved)],
                         [q_ref[rows, :] for rows in chunks], [k_ref[rows, :] for rows in chunks],
                         [v_ref[rows, :] for rows in chunks], [_lane_col(gt, LANE_G + h) for gt in gts],
                         [_lane_col(gt, LANE_BETA + h) for gt in gts])
        dqs, dks, dvs, dgcs, dbcs = vjp([(du_ref[rows, :], dw_ref[rows, :], da_ref[rows, :], dqd_ref[rows, :],
                                          dkd_ref[rows, :], dgl_ref[cidx][:, 0:1]) for cidx, rows in enumerate(chunks)])
        for cidx, rows in enumerate(chunks):
            dq, dk, dv, dgc, dbc = dqs[cidx], dks[cidx], dvs[cidx], dgcs[cidx], dbcs[cidx]
            dqkv_ref[0, rows, :] = dq
            dqkv_ref[1, rows, :] = dk
            dqkv_ref[2, rows, :] = dv
            dg_ref[rows, :] += jnp.where(lane == LANE_G + h, dgc, 0.0) + jnp.where(lane == LANE_BETA + h, dbc, 0.0)

        if exchange is not None:
            @pl.when((pl.program_id(0) == NCB - 1) & (h == NGH - 1))
            def _():
                exchange.finish(*exchange.split(ex_refs))

    qs, ks, vs, gs, a_s, gl_s = _prep_specs()
    res = pl.pallas_call(
        body, name="gdn_prep_bwd", grid=(NCB, NGH), in_specs=[qs, ks, vs, gs, a_s, qs, qs, qs, qs, a_s, gl_s] + ex_in_specs,
        out_specs=[pl.BlockSpec((3, CB * CHUNK, GHD), lambda i, h: (0, i, h)), gs] + ex_out_specs,
        out_shape=[jax.ShapeDtypeStruct((3, S, DGDN), F32), jax.ShapeDtypeStruct((S, LANES), F32)] + ex_out_shape,
        scratch_shapes=ex_scratch, compiler_params=_cparams(),
    )(qkv, qkv, qkv, gates, t_inv, du, dw, dqd, dkd, da, dgl, *ex_in)
    return res[0], res[1], res[2:]


def _scan_specs(nh, parts, reverse):
    wide, rows, chunks = nh * GHD, S // parts, NCH // parts

    def part(p):
        return parts - 1 - p if reverse else p

    hs = pl.BlockSpec((rows, wide), lambda g, p: (part(p), g))
    a_s = pl.BlockSpec((nh, rows, CHUNK), lambda g, p: (g, part(p), 0))
    gl_s = pl.BlockSpec((nh, chunks, 1, LANES), lambda g, p: (g, part(p), 0, 0))
    st_s = pl.BlockSpec((nh, chunks, GHD, GHD), lambda g, p: (g, part(p), 0, 0))
    gz_s = pl.BlockSpec((rows, wide), lambda g, p: (part(p), BLK_GZ // nh + g))
    mix_s = pl.BlockSpec((rows, wide), lambda g, p: (part(p), NPAIR // nh + g))
    return hs, a_s, gl_s, st_s, gz_s, mix_s


def _head_cols(hh):
    return slice(hh * GHD, (hh + 1) * GHD)


SCAN_HEADS, SCAN_PARTS = 4, 2
SCAN_HEADS_BWD, SCAN_PARTS_BWD = 4, 4


def _gdn_scan(u, w, qd, kd, a, gl, proj, w_norm, mix):
    heads = range(SCAN_HEADS)

    def body(u_ref, w_ref, qd_ref, kd_ref, a_ref, gl_ref, z_ref, wn_ref, mix_in, mix_ref, o_ref, st_ref, carry_ref):
        del mix_in

        @pl.when(pl.program_id(1) == 0)
        def _():
            carry_ref[...] = jnp.zeros_like(carry_ref)

        def step(ci, states):
            rows = pl.ds(pl.multiple_of(ci * CHUNK, CHUNK), CHUNK)
            for hh in heads:
                st_ref[hh, ci] = states[hh]
            ws = [_dot(w_ref[rows, _head_cols(hh)], states[hh]) for hh in heads]
            qs = [_dot(qd_ref[rows, _head_cols(hh)], states[hh]) for hh in heads]
            vn = [u_ref[rows, _head_cols(hh)] - ws[hh] for hh in heads]
            av = [_dot(a_ref[hh, rows, :], vn[hh]) for hh in heads]
            kv = [_dot(kd_ref[rows, _head_cols(hh)], vn[hh], 0, 0) for hh in heads]
            for hh in heads:
                o_ref[rows, _head_cols(hh)] = qs[hh] + av[hh]
            return tuple(states[hh] * gl_ref[hh, ci] + kv[hh] for hh in heads)

        last = lax.fori_loop(0, NCH // SCAN_PARTS, step, tuple(carry_ref[hh] for hh in heads))
        for hh in heads:
            carry_ref[hh] = last[hh]
            ov = o_ref[:, _head_cols(hh)]
            mix_ref[:, _head_cols(hh)] = (ov * _rms_scale(ov) * wn_ref[...] * _silu(z_ref[:, _head_cols(hh)])).astype(BF16)

    hs, a_s, gl_s, st_s, gz_s, mix_s = _scan_specs(SCAN_HEADS, SCAN_PARTS, False)
    return pl.pallas_call(
        body, name="gdn_scan", grid=(NGH // SCAN_HEADS, SCAN_PARTS),
        in_specs=[hs, hs, hs, hs, a_s, gl_s, gz_s, pl.BlockSpec((1, GHD), lambda g, p: (0, 0)),
                  pl.BlockSpec(memory_space=pl.ANY)],
        out_specs=[mix_s, hs, st_s],
        out_shape=[jax.ShapeDtypeStruct((S, D), BF16), jax.ShapeDtypeStruct((S, DGDN), F32),
                   jax.ShapeDtypeStruct((NGH, NCH, GHD, GHD), F32)],
        input_output_aliases={8: 0}, scratch_shapes=[pltpu.VMEM((SCAN_HEADS, GHD, GHD), F32)], compiler_params=_cparams(),
    )(u, w, qd, kd, a, gl, proj, w_norm, mix)


def _gdn_scan_bwd(dmix, o, proj, w_norm, u, w, qd, kd, a, gl, states, dproj, exchange=None):
    ex_in, ex_in_specs, ex_out_specs, ex_out_shape, ex_scratch = _hosted(exchange)
    groups = NGH // SCAN_HEADS_BWD

    def body(*refs):
        dy_ref, o_ref, z_ref, wn_ref, u_ref, w_ref, qd_ref, kd_ref, a_ref, gl_ref, st_ref = refs[:11]
        dz_ref, du_ref, dw_ref, dqd_ref, dkd_ref, da_ref, dgl_ref, dwn_ref = refs[12 + len(ex_in):20 + len(ex_in)]
        do_ref, carry_ref = refs[20 + len(ex_in) + len(ex_out_shape):22 + len(ex_in) + len(ex_out_shape)]
        ex_refs = refs[12:12 + len(ex_in)] + refs[20 + len(ex_in):20 + len(ex_in) + len(ex_out_shape)] + refs[-2:]
        heads = range(SCAN_HEADS_BWD)
        chunks = NCH // SCAN_PARTS_BWD

        if exchange is not None:
            @pl.when((pl.program_id(0) == 0) & (pl.program_id(1) == 0))
            def _():
                exchange.start(*exchange.split(ex_refs))

        @pl.when((pl.program_id(0) == 0) & (pl.program_id(1) == 0))
        def _():
            dwn_ref[...] = jnp.zeros_like(dwn_ref)

        @pl.when(pl.program_id(1) == 0)
        def _():
            carry_ref[...] = jnp.zeros_like(carry_ref)

        wn = wn_ref[...]
        for hh in heads:
            c = _head_cols(hh)
            ov = o_ref[:, c]
            zv = z_ref[:, c]
            g = dy_ref[:, c]
            sig = _sigmoid(zv)
            dz_ref[:, c] = (g * (ov * _rms_scale(ov) * wn) * sig * (1.0 + zv * (1.0 - sig))).astype(BF16)
            do, dwt = _rms_bwd(ov, wn, g * zv * sig)
            do_ref[:, c] = do
            dwn_ref[...] += jnp.sum(dwt, axis=0, keepdims=True)

        def step(t, dstates):
            ci = chunks - 1 - t
            rows = pl.ds(pl.multiple_of(ci * CHUNK, CHUNK), CHUNK)
            cols = [_head_cols(hh) for hh in heads]
            state = [st_ref[hh, ci] for hh in heads]
            dov = [do_ref[rows, cols[hh]] for hh in heads]
            wv = [w_ref[rows, cols[hh]] for hh in heads]
            ws = [_dot(wv[hh], state[hh]) for hh in heads]
            adov = [_dot(a_ref[hh, rows, :], dov[hh], 0, 0) for hh in heads]
            kds = [_dot(kd_ref[rows, cols[hh]], dstates[hh]) for hh in heads]
            dqd = [_dot(dov[hh], state[hh], 1, 1) for hh in heads]
            qdo = [_dot(qd_ref[rows, cols[hh]], dov[hh], 0, 0) for hh in heads]
            vn = [u_ref[rows, cols[hh]] - ws[hh] for hh in heads]
            dvn = [adov[hh] + kds[hh] for hh in heads]
            da = [_dot(dov[hh], vn[hh], 1, 1) for hh in heads]
            dkd = [_dot(vn[hh], dstates[hh], 1, 1) for hh in heads]
            dwv = [_dot(dvn[hh], state[hh], 1, 1) for hh in heads]
            wdv = [_dot(wv[hh], dvn[hh], 0, 0) for hh in heads]
            for hh in heads:
                da_ref[hh, rows, :] = da[hh]
                dqd_ref[rows, cols[hh]] = dqd[hh]
                dkd_ref[rows, cols[hh]] = dkd[hh]
                dgl = jnp.sum(jnp.sum(dstates[hh] * state[hh], axis=1, keepdims=True), axis=0, keepdims=True)
                dgl_ref[hh, ci] = jnp.broadcast_to(dgl, (1, LANES))
                du_ref[rows, cols[hh]] = dvn[hh]
                dw_ref[rows, cols[hh]] = -dwv[hh]
            return tuple(dstates[hh] * gl_ref[hh, ci] + qdo[hh] - wdv[hh] for hh in heads)

        last = lax.fori_loop(0, chunks, step, tuple(carry_ref[hh] for hh in heads))
        for hh in heads:
            carry_ref[hh] = last[hh]

        if exchange is not None:
            @pl.when((pl.program_id(0) == groups - 1) & (pl.program_id(1) == SCAN_PARTS_BWD - 1))
            def _():
                exchange.finish(*exchange.split(ex_refs))

    hs, a_s, gl_s, st_s, gz_s, mix_s = _scan_specs(SCAN_HEADS_BWD, SCAN_PARTS_BWD, True)
    vec = pl.BlockSpec((1, GHD), lambda g, p: (0, 0))
    tok = jax.ShapeDtypeStruct((S, DGDN), F32)
    res = pl.pallas_call(
        body, name="gdn_scan_bwd", grid=(groups, SCAN_PARTS_BWD),
        in_specs=[mix_s, hs, gz_s, vec, hs, hs, hs, hs, a_s, gl_s, st_s, pl.BlockSpec(memory_space=pl.ANY)] + ex_in_specs,
        out_specs=[gz_s, hs, hs, hs, hs, a_s, gl_s, vec] + ex_out_specs,
        out_shape=[jax.ShapeDtypeStruct((S, DPROJ_PAD), BF16), tok, tok, tok, tok,
                   jax.ShapeDtypeStruct((NGH, S, CHUNK), F32), jax.ShapeDtypeStruct((NGH, NCH, 1, LANES), F32),
                   jax.ShapeDtypeStruct((1, GHD), F32)] + ex_out_shape,
        input_output_aliases={11: 0},
        scratch_shapes=[pltpu.VMEM((S // SCAN_PARTS_BWD, SCAN_HEADS_BWD * GHD), F32),
                        pltpu.VMEM((SCAN_HEADS_BWD, GHD, GHD), F32)] + ex_scratch,
        compiler_params=_cparams(),
    )(dmix, o, proj, w_norm, u, w, qd, kd, a, gl, states, dproj, *ex_in)
    return res[:8], res[8:]


def _place():
    return lax.axis_index("x"), lax.axis_index("y"), lax.axis_index("c")


def _place_scalars():
    x, y, c = _place()
    return jnp.stack([2 * x + y, c]).astype(jnp.int32)


def _other_chips(x, y):
    return [(1 - x, y), (x, 1 - y), (1 - x, 1 - y)]


HBM = pl.BlockSpec(memory_space=pltpu.HBM)
VMEM = pl.BlockSpec(memory_space=pltpu.VMEM)


def _half_rows(ref_or_rows, half):
    rows = ref_or_rows // 2
    return pl.ds(pl.multiple_of(half * rows, rows), rows)


class _Exchange:
    def __init__(self, inputs, out_shape, n_sems, start, finish=None, middle=None, rest=None):
        self.inputs, self.out_shape, self.n_sems, self.start = inputs, out_shape, n_sems, start
        if finish is None:
            def finish(*refs):
                middle(*refs)
                rest(*refs)
        self.finish = finish
        self.middle = middle if middle is not None else (lambda *refs: None)
        self.rest = rest if rest is not None else finish

    def sem_shapes(self):
        return [pltpu.SemaphoreType.DMA((self.n_sems,)), pltpu.SemaphoreType.DMA((self.n_sems,))]

    def split(self, refs):
        n_in, n_out = len(self.inputs), len(self.out_shape)
        return refs[:n_in], refs[n_in:n_in + n_out], refs[n_in + n_out], refs[n_in + n_out + 1]


def _allgather_exchange(shards, whole=(), sent_rows=None):
    n, nw = len(shards), len(whole)
    slots = 8
    sent_rows = [s.shape[0] for s in shards] if sent_rows is None else sent_rows

    def plan(c, src, outs, send_sems, recv_sems):
        x, y, _ = _place()
        via_x, via_y, diagonal = _other_chips(x, y)
        id_x, id_y, id_diagonal = [2 * chip[0] + chip[1] for chip in (via_x, via_y, diagonal)]
        me, sibling = (x, y, c), (x, y, 1 - c)

        def rows_of(a, half, quarter):
            total = src[a].shape[0]
            first = half * (total // 2) + (0 if quarter is None else quarter * (total // 4))
            size = min(total // 2 if quarter is None else total // 4, sent_rows[a] - first)
            assert size > 0 and size % BF16_ROWS == 0, (a, half, quarter, size)
            return pl.ds(first, size)

        def copy(a, k, chip_index, half, quarter, to, from_src=False):
            rows = rows_of(a, half, quarter)
            dst = outs[a].at[chip_index, rows]
            return pltpu.make_async_remote_copy(
                src_ref=src[a].at[rows] if from_src else dst, dst_ref=dst, send_sem=send_sems.at[slots * a + k],
                recv_sem=recv_sems.at[slots * a + k], device_id=to, device_id_type=MESH)

        def whole_copy(b, k, chip_index, to):
            return pltpu.make_async_remote_copy(
                src_ref=src[n + b], dst_ref=outs[n + b].at[chip_index], send_sem=send_sems.at[slots * n + 3 * b + k],
                recv_sem=recv_sems.at[slots * n + 3 * b + k], device_id=to, device_id_type=MESH)

        first, stages, last = [], [], []
        for a in range(n):
            first += [copy(a, 0, 2 * x + y, c, None, (*via_x, c), True), copy(a, 1, 2 * x + y, c, None, (*via_y, c), True)]
            stages.append([
                (copy(a, 0, id_x, c, None, me),
                 [copy(a, 2, id_x, c, 0, (*via_y, c)), copy(a, 4, id_x, c, None, sibling)]),
                (copy(a, 1, id_y, c, None, me),
                 [copy(a, 3, id_y, c, 1, (*via_x, c)), copy(a, 5, id_y, c, None, sibling)]),
                (copy(a, 2, id_diagonal, c, 0, me), [copy(a, 6, id_diagonal, c, 0, sibling)]),
                (copy(a, 3, id_diagonal, c, 1, me), [copy(a, 7, id_diagonal, c, 1, sibling)]),
            ])
            last += [copy(a, 4, id_x, 1 - c, None, me), copy(a, 5, id_y, 1 - c, None, me),
                     copy(a, 6, id_diagonal, 1 - c, 0, me), copy(a, 7, id_diagonal, 1 - c, 1, me)]
        for b in range(nw):
            for k, (chip, index) in enumerate(((via_x, id_x), (via_y, id_y), (diagonal, id_diagonal))):
                first.append(whole_copy(b, k, 2 * x + y, (*chip, c)))
                last.append(whole_copy(b, k, index, me))
        return first, stages, last

    def each_core(step):
        def run(*refs):
            for core in (0, 1):
                @pl.when(lax.axis_index("c") == core)
                def _(core=core):
                    step(core, *refs)

        return run

    def start(core, *refs):
        for cp in plan(core, *refs)[0]:
            cp.start()

    def pass_on(stages, which):
        for stage in which:
            for per_shard in stages:
                lands, onward = per_shard[stage]
                lands.wait_recv()
                for cp in onward:
                    cp.start()

    def middle(core, *refs):
        pass_on(plan(core, *refs)[1], (0, 1))

    def rest(core, *refs):
        first, stages, last = plan(core, *refs)
        pass_on(stages, (2, 3))
        for cp in last:
            cp.wait_recv()
        for cp in first + [cp for per_shard in stages for _, onward in per_shard for cp in onward]:
            cp.wait_send()

    out_shape = [jax.ShapeDtypeStruct((NCHIP,) + s.shape, s.dtype) for s in list(shards) + list(whole)]
    return _Exchange(list(shards) + list(whole), out_shape, slots * n + 3 * nw, each_core(start),
                     middle=each_core(middle), rest=each_core(rest))


def _with_own(gathered, own):
    x, y, _ = _place()
    return lax.dynamic_update_index_in_dim(gathered, own, 2 * x + y, axis=0)


def _simple_exchange(inputs, out_shape, copies_of):
    def start(*refs):
        for cp in copies_of(*refs):
            cp.start()

    def finish(*refs):
        for cp in copies_of(*refs):
            cp.wait()

    return _Exchange(list(inputs), out_shape, len(out_shape) * 3, start, finish)


def _pair_exchange(grads):
    def copies_of(src, outs, send_sems, recv_sems):
        x, y, c = _place()
        return [pltpu.make_async_remote_copy(
            src_ref=src[a].at[:, _half_rows(src[a].shape[1], 1 - c)], dst_ref=outs[a], send_sem=send_sems.at[a],
            recv_sem=recv_sems.at[a], device_id=(x, y, 1 - c), device_id_type=MESH) for a in range(len(src))]

    return _simple_exchange(
        grads, [jax.ShapeDtypeStruct((g.shape[0], g.shape[1] // 2, g.shape[2]), g.dtype) for g in grads], copies_of)


def _pair_sum(grads, theirs, name):
    n = len(grads)

    def body(place_ref, *refs):
        for a in range(n):
            refs[2 * n + a][...] = (refs[a][...].astype(F32) + refs[n + a][...].astype(F32)).astype(BF16)

    def specs(arrs):
        return [pl.BlockSpec((None,) + g.shape[1:], lambda j, place: (j, 0, 0)) for g in arrs]

    own_half = [pl.BlockSpec((None, g.shape[1] // 2, g.shape[2]), lambda j, place: (j, place[1], 0)) for g in grads]
    return pl.pallas_call(
        body, name=name, grid_spec=pltpu.PrefetchScalarGridSpec(
            num_scalar_prefetch=1, grid=(NCHIP,), in_specs=own_half + specs(theirs), out_specs=specs(theirs)),
        out_shape=[jax.ShapeDtypeStruct(g.shape, BF16) for g in theirs], compiler_params=_cparams(),
    )(_place_scalars(), *grads, *theirs)


def _chip_exchange(parts, rows_by_core=None):
    if rows_by_core is None:
        rows_by_core = [(p.shape[1],) * 2 for p in parts]

    def each_core(act):
        def run(src, outs, send_sems, recv_sems):
            x, y, c = _place()
            for core in (0, 1):
                @pl.when(c == core)
                def _(core=core):
                    for a in range(len(src)):
                        rows = pl.ds(0, rows_by_core[a][core])
                        for k, chip in enumerate(_other_chips(x, y)):
                            act(pltpu.make_async_remote_copy(
                                src_ref=src[a].at[2 * chip[0] + chip[1], rows], dst_ref=outs[a].at[k, rows],
                                send_sem=send_sems.at[3 * a + k], recv_sem=recv_sems.at[3 * a + k],
                                device_id=(*chip, core), device_id_type=MESH))

        return run

    return _Exchange(list(parts), [jax.ShapeDtypeStruct((NCHIP - 1,) + p.shape[1:], p.dtype) for p in parts],
                     3 * len(parts), each_core(lambda cp: cp.start()), each_core(lambda cp: cp.wait()))


def _chip_sum(parts, received, exchange=None):
    n = len(parts)
    steps = 4
    ex_in, ex_in_specs, ex_out_specs, ex_out_shape, ex_scratch = _hosted(exchange)
    n_ex = len(ex_in)

    def body(place_ref, *refs):
        mine, theirs = refs[2 * n + n_ex:3 * n + n_ex], refs[3 * n + n_ex:4 * n + n_ex]
        ex_refs = refs[2 * n:2 * n + n_ex] + refs[4 * n + n_ex:len(refs) - n - 2]
        tiles, send_sems, recv_sems = refs[len(refs) - n - 2:len(refs) - 2], refs[-2], refs[-1]
        step = pl.program_id(0)
        if exchange is not None:
            @pl.when(step == 0)
            def _():
                exchange.start(*exchange.split(ex_refs))

        x, y, c = _place()

        def share(a, i):
            rows = tiles[a].shape[1]
            return pltpu.make_async_remote_copy(
                src_ref=tiles[a].at[i], dst_ref=theirs[a].at[pl.ds(pl.multiple_of(i * rows, rows), rows)],
                send_sem=send_sems.at[a * steps + i], recv_sem=recv_sems.at[a * steps + i], device_id=(x, y, 1 - c),
                device_id_type=MESH)

        for a in range(n):
            own, r = refs[a], refs[n + a]
            total = ((own[...].astype(F32) + r[0].astype(F32)) + r[1].astype(F32)) + r[2].astype(F32)
            mine[a][...] = total
            tiles[a][step] = total
            share(a, step).start()

        @pl.when(step == steps - 1)
        def _():
            if exchange is not None:
                exchange.finish(*exchange.split(ex_refs))
            for a in range(n):
                for i in range(steps):
                    share(a, i).wait()

    own_specs = [pl.BlockSpec((None, g.shape[1] // steps, g.shape[2]), lambda i, place: (place[0], i, 0)) for g in parts]
    received_specs = [pl.BlockSpec((g.shape[0], g.shape[1] // steps, g.shape[2]), lambda i, place: (0, i, 0))
                      for g in received]
    out_specs = [pl.BlockSpec((g.shape[1] // steps, g.shape[2]), lambda i, place: (i, 0)) for g in parts]
    halves = [jax.ShapeDtypeStruct(g.shape[1:], F32) for g in parts]
    res = pl.pallas_call(
        body, name="grads_chip_sum", grid_spec=pltpu.PrefetchScalarGridSpec(
            num_scalar_prefetch=1, grid=(steps,), in_specs=own_specs + received_specs + ex_in_specs,
            out_specs=out_specs + [HBM] * n + ex_out_specs,
            scratch_shapes=ex_scratch + [pltpu.VMEM((steps, g.shape[1] // steps, g.shape[2]), F32) for g in parts]
            + [pltpu.SemaphoreType.DMA((n * steps,))] * 2),
        out_shape=halves * 2 + ex_out_shape, compiler_params=_cparams(),
    )(_place_scalars(), *parts, *received, *ex_in)
    return res[:n], res[n:2 * n], res[2 * n:]


def _adamw_math(w, g, m, v):
    nm = ADAM_B1 * m + (1.0 - ADAM_B1) * g
    nv = ADAM_B2 * v + (1.0 - ADAM_B2) * jnp.square(g)
    m_hat = nm / (1.0 - ADAM_B1 ** ADAM_STEP)
    v_hat = nv / (1.0 - ADAM_B2 ** ADAM_STEP)
    return -ADAM_LR * (m_hat / (jnp.sqrt(v_hat) + ADAM_EPS) + ADAM_WD * w), nm, nv


def _adamw_big(ws, g_mine, g_theirs, ms, vs):
    n = len(ws)
    steps = 8
    per_half = steps // 2

    def body(place_ref, *refs):
        outs = refs[5 * n:]
        own_half = (pl.program_id(0) // per_half) == place_ref[1]
        for a in range(n):
            g = jnp.where(own_half, refs[n + a][...], refs[2 * n + a][...])
            d, nm, nv = _adamw_math(refs[a][...], g, refs[3 * n + a][...], refs[4 * n + a][...])
            outs[a][...] = g
            outs[n + a][...] = d
            outs[2 * n + a][...] = nm
            outs[3 * n + a][...] = nv

    specs = [pl.BlockSpec((w.shape[0] // steps, w.shape[1]), lambda i, place: (i, 0)) for w in ws]

    def half_specs(halves, of_this_core):
        def tile(i, place):
            first = (place[1] if of_this_core else 1 - place[1]) * per_half
            return jnp.clip(i - first, 0, per_half - 1), 0

        return [pl.BlockSpec((g.shape[0] // per_half, g.shape[1]), tile) for g in halves]

    shapes = [jax.ShapeDtypeStruct(w.shape, F32) for w in ws]
    res = pl.pallas_call(
        body, name="adamw_big", grid_spec=pltpu.PrefetchScalarGridSpec(
            num_scalar_prefetch=1, grid=(steps,),
            in_specs=specs + half_specs(g_mine, True) + half_specs(g_theirs, False) + specs * 2, out_specs=specs * 4),
        out_shape=shapes * 4, compiler_params=_cparams(),
    )(_place_scalars(), *ws, *g_mine, *g_theirs, *ms, *vs)
    return res[:n], res[n:2 * n], res[2 * n:3 * n], res[3 * n:]


def _adamw_in(w, g_mine, g_theirs, m, v):
    half = D // 2

    def body(w_ref, gm_ref, gt_ref, m_ref, v_ref, g_out, d_out, nm_out, nv_out, g_ref):
        south = lax.axis_index("c") == 0
        g_ref[0:half, :] = jnp.where(south, gm_ref[...], gt_ref[...])
        g_ref[half:D, :] = jnp.where(south, gt_ref[...], gm_ref[...])
        g = g_ref[0:CW, :]
        d, nm, nv = _adamw_math(w_ref[...], g, m_ref[...], v_ref[...])
        g_out[...] = g
        d_out[...] = d
        nm_out[...] = nm
        nv_out[...] = nv

    cols = 4 * LANES
    spec = pl.BlockSpec((CW, cols), lambda i: (0, i))
    half_spec = pl.BlockSpec((half, cols), lambda i: (0, i))
    return pl.pallas_call(
        body, name="adamw_in", grid=(D // cols,), in_specs=[spec, half_spec, half_spec, spec, spec], out_specs=[spec] * 4,
        out_shape=[jax.ShapeDtypeStruct((CW, D), F32)] * 4, scratch_shapes=[pltpu.VMEM((D, cols), F32)],
        compiler_params=_cparams(),
    )(w, g_mine, g_theirs, m, v)


NORM_NAMES = ("pre_mix_norm", "post_mix_norm", "pre_mlp_norm", "post_mlp_norm")
SMALL_NAMES = NORM_NAMES + ("gdn_conv_w", "fox_f_bias", "gdn_dt_bias", "gdn_a_log", "fox_out_norm", "gdn_out_norm")
CONV_COLS = 3 * DGDN // NCHIP


def _small_gather(d_norms, d_conv, sums, d_fox_norm, d_gdn_norm, loss_row):
    n_arrays = 6
    n_remote = n_arrays * (NDEV - 1)

    def copies_of(src, outs, send_sems, recv_sems):
        x, y, c = _place()
        me = 4 * x + 2 * y + c

        def from_me(chip_index):
            cols = pl.ds(pl.multiple_of(chip_index * CONV_COLS, LANES), CONV_COLS)
            return [src[0], src[1].at[:, cols], src[2], src[3], src[4], src[5]]

        local = [pltpu.make_async_copy(s, outs[a].at[me], send_sems.at[n_remote + a]) for a, s in enumerate(from_me(2 * x + y))]
        remote = []
        for k in range(1, NDEV):
            px, py, pc = x ^ ((k >> 2) & 1), y ^ ((k >> 1) & 1), c ^ (k & 1)
            remote += [pltpu.make_async_remote_copy(
                src_ref=s, dst_ref=outs[a].at[me], send_sem=send_sems.at[n_arrays * (k - 1) + a],
                recv_sem=recv_sems.at[n_arrays * (k - 1) + a], device_id=(px, py, pc), device_id_type=MESH)
                for a, s in enumerate(from_me(2 * px + py))]
        return local + remote

    def start(*refs):
        for cp in copies_of(*refs):
            cp.start()

    def finish(*refs):
        for cp in copies_of(*refs):
            cp.wait()

    shapes = [(4, D), (CONV_K, CONV_COLS), (8, LANES), (1, LANES), (1, LANES), (1, LANES)]
    return _Exchange([d_norms, d_conv, sums, d_fox_norm, d_gdn_norm, loss_row],
                     [jax.ShapeDtypeStruct((NDEV,) + s, F32) for s in shapes], n_remote + n_arrays, start, finish)


def _small_adamw(gathered, ws, ms, vs):
    n = len(SMALL_NAMES)
    ng = len(gathered)

    def body(*refs):
        def total(buf):
            acc = buf[0]
            for i in range(1, NDEV):
                acc = acc + buf[i]
            return acc

        t_norms, t_conv, t_sums, t_fn, t_gn, t_loss = [total(r) for r in refs[:ng]]
        w_refs, m_refs, v_refs = refs[ng:ng + n], refs[ng + n:ng + 2 * n], refs[ng + 2 * n:ng + 3 * n]
        outs = refs[ng + 3 * n:]
        outs[4 * n][...] = t_loss
        grads = [t_norms[i:i + 1, :] for i in range(4)] + [
            t_conv, t_sums[0:1, 0:NFH], t_sums[1:2, 0:NGH], t_sums[2:3, 0:NGH], t_fn[:, 0:FHD], t_gn]
        for a in range(n):
            d, nm, nv = _adamw_math(w_refs[a][...], grads[a], m_refs[a][...], v_refs[a][...])
            outs[a][...] = grads[a]
            outs[n + a][...] = d
            outs[2 * n + a][...] = nm
            outs[3 * n + a][...] = nv

    def whole(arr):
        return pl.BlockSpec(arr.shape, lambda i: (0,) * arr.ndim)

    res = pl.pallas_call(
        body, name="small_adamw", grid=(1,), in_specs=[whole(t) for t in gathered] + [whole(w) for w in ws] * 3,
        out_specs=[whole(w) for w in ws] * 4 + [pl.BlockSpec((1, LANES), lambda i: (0, 0))],
        out_shape=[jax.ShapeDtypeStruct(w.shape, F32) for w in ws] * 4 + [jax.ShapeDtypeStruct((1, LANES), F32)],
        compiler_params=_cparams(),
    )(*gathered, *ws, *ms, *vs)
    return res[:n], res[n:2 * n], res[2 * n:3 * n], res[3 * n:4 * n], res[4 * n]


CW = DPROJ // NCHIP
BF16_ROWS = 16
CW_SENT = -(-CW // BF16_ROWS) * BF16_ROWS
PROJ_RUNS = tuple((part * DFOX + hp * LANES, part * DFOX + (hp + 1) * LANES, (3 * hp + part) * LANES)
                  for hp in range(NPAIR) for part in range(3)) + (
    (1536, 1544, BLK_SMALL * LANES), (1544, 3080, BLK_GDN * LANES), (3080, 3088, BLK_SMALL * LANES + 8),
    (3088, 3600, BLK_GZ * LANES))


def _proj_pieces():
    pieces = []
    for lo, hi, at in PROJ_RUNS:
        while lo < hi:
            j = lo // CW
            end = min(hi, (j + 1) * CW)
            pieces.append((j, lo - j * CW, at, end - lo))
            at, lo = at + end - lo, end
    return pieces


RT = 256


def _to_padded_rows(gathered):
    def body(src_ref, out_ref, blocks_ref, rows_ref):
        blocks_ref[...] = src_ref[...].astype(F32)
        rows_ref[...] = jnp.zeros_like(rows_ref)
        for j, start, at, n in _proj_pieces():
            rows_ref[at:at + n, :] = blocks_ref[j, start:start + n, :]
        out_ref[...] = rows_ref[...].astype(out_ref.dtype)

    return pl.pallas_call(
        body, name="proj_rows_in", grid=(D // RT,), in_specs=[pl.BlockSpec((NCHIP, D, RT), lambda i: (0, 0, i))],
        out_specs=pl.BlockSpec((DPROJ_PAD, RT), lambda i: (0, i)), out_shape=jax.ShapeDtypeStruct((DPROJ_PAD, D), gathered.dtype),
        scratch_shapes=[pltpu.VMEM((NCHIP, D, RT), F32), pltpu.VMEM((DPROJ_PAD, RT), F32)], compiler_params=_cparams(),
    )(gathered)


def _from_padded_rows_pair_sum(w):
    steps = D // RT
    half = D // 2

    def body(src_ref, out_ref, rows_ref, blocks_ref, mine_ref, send_ref, recv_ref, send_sems, recv_sems):
        i = pl.program_id(0)
        x, y, c = _place()

        def share(t):
            return pltpu.make_async_remote_copy(
                src_ref=send_ref.at[t], dst_ref=recv_ref.at[t], send_sem=send_sems.at[t], recv_sem=recv_sems.at[t],
                device_id=(x, y, 1 - c), device_id_type=MESH)

        def rows_of(core):
            return blocks_ref[:, pl.ds(pl.multiple_of(core * half, half), half), :]

        @pl.when(i < steps)
        def _():
            rows_ref[...] = src_ref[...].astype(F32)
            blocks_ref[...] = jnp.zeros_like(blocks_ref)
            for j, start, at, n in _proj_pieces():
                blocks_ref[j, start:start + n, :] = rows_ref[at:at + n, :]
            mine_ref[i % 2] = rows_of(c)
            send_ref[i] = rows_of(1 - c).astype(BF16)
            share(i).start()

        @pl.when(i > 0)
        def _():
            share(i - 1).wait_recv()
            out_ref[...] = (mine_ref[(i - 1) % 2] + recv_ref[i - 1].astype(F32)).astype(BF16)

        @pl.when(i == steps)
        def _():
            for t in range(steps):
                share(t).wait_send()

    tile = (NCHIP, half, RT)
    return pl.pallas_call(
        body, name="proj_rows_out_pair_sum", grid=(steps + 1,),
        in_specs=[pl.BlockSpec((DPROJ_PAD, RT), lambda i: (0, jnp.minimum(i, steps - 1)))],
        out_specs=pl.BlockSpec(tile, lambda i: (0, 0, jnp.maximum(i - 1, 0))),
        out_shape=jax.ShapeDtypeStruct((NCHIP, half, D), BF16),
        scratch_shapes=[pltpu.VMEM((DPROJ_PAD, RT), F32), pltpu.VMEM((NCHIP, D, RT), F32), pltpu.VMEM((2,) + tile, F32),
                        pltpu.VMEM((steps,) + tile, BF16), pltpu.VMEM((steps,) + tile, BF16),
                        pltpu.SemaphoreType.DMA((steps,)), pltpu.SemaphoreType.DMA((steps,))],
        compiler_params=_cparams(),
    )(w)


def _local_step(x, target, first_weights, late_weights, reduce_late, reduce_in, pre_mix_norm, fox_f_bias, fox_out_norm,
                gdn_a_log, gdn_dt_bias, gdn_out_norm, post_mix_norm, pre_mlp_norm, post_mlp_norm):
    bias_vec = jnp.zeros((1, LANES), F32).at[0, 0:NFH].set(fox_f_bias).at[0, LANE_G:LANE_G + NGH].set(gdn_dt_bias)
    alog_vec = jnp.zeros((1, LANES), F32).at[0, LANE_G:LANE_G + NGH].set(gdn_a_log)
    w2 = jnp.concatenate([fox_out_norm, fox_out_norm], axis=1)

    h, first = _pre_norm(x, pre_mix_norm, exchange=first_weights[0])
    win_p, conv_w = first_weights[1](first)
    proj = _matmul(h, win_p, tb=True, tm=2048, tn=768, tk=1024, name="mm_proj", exchange=late_weights[0])
    proj, late_a = proj if late_weights[0] is not None else (proj, [])
    gates = _gates(proj, bias_vec, alog_vec)
    mix, fox_o, lse, late_b = _fox_fwd(proj, gates, w2, exchange=late_weights[1])
    qkv = _gdn_pre(proj, conv_w)
    (u, w, qd, kd, a_intra, gl, t_inv), _ = _gdn_prep(qkv, gates)
    wout, wup3 = late_weights[3](late_a, late_b)
    mix, gdn_raw, states = _gdn_scan(u, w, qd, kd, a_intra, gl, proj, gdn_out_norm, mix)
    def post_mix(acc, xv, w_post, w_pre_mlp):
        x1v = xv + acc * _rms_scale(acc) * w_post
        return acc, x1v, x1v * _rms_scale(x1v) * w_pre_mlp

    mixed, x1, h2 = _matmul(mix, wout, tm=512, tn=D, tk=1024, out_dtypes=(F32, F32, BF16), name="mm_out",
                            extra=(x, post_mix_norm, pre_mlp_norm), epilogue=post_mix)

    def relu2(acc):
        r = jnp.maximum(acc, 0.0)
        return r, r * r

    up_act = _matmul(h2, wup3, b3=True, tm=1024, tn=1024, tk=1024, out_dtypes=(BF16, BF16), epilogue=relu2,
                     name="mm_up", exchange=late_weights[2])
    (up_relu, act), late_c = up_act if late_weights[2] is not None else (up_act, [])
    wdown = late_weights[4](late_c)
    def loss_head(acc, x1v, tv, w):
        err = x1v + acc * _rms_scale(acc) * w - tv
        dx2v = err * (1.0 / D)
        dyv, dwt = _rms_bwd(acc, w, dx2v)
        part = 0.5 * jnp.sum(jnp.mean(err * err, axis=-1, keepdims=True), axis=0, keepdims=True)
        return dx2v, dyv, jnp.sum(dwt, axis=0, keepdims=True), jnp.broadcast_to(part, (1, D))

    dx2, dy, d_post_mlp, loss_wide = _matmul(
        act, wdown, tm=512, tn=D, tk=DFF, out_dtypes=(F32, BF16, F32, F32), extra=(x1, target, post_mlp_norm),
        epilogue=loss_head, n_sums=2, name="mm_down")
    loss_row = loss_wide[:, :LANES]

    dwdown = _matmul(act, dy, ta=True, tm=1024, tn=1024, tk=2048, out_dtypes=(BF16,), name="mm_dwdown")

    def relu2_bwd(acc, r):
        return (acc * 2.0 * r.astype(F32),)

    dup = _matmul(dy, wdown, tb=True, tm=1024, tn=1024, tk=1024, out_dtypes=(BF16,), extra=(up_relu,), epilogue=relu2_bwd,
                  name="mm_dact")
    dwup3 = _matmul(h2, dup, ta=True, tm=1024, tn=1024, tk=2048, out_dtypes=(BF16,), o3=True, name="mm_dwup")
    def mid_bwd(acc, x1v, dx2v, mixedv, w_pre_mlp, w_post):
        dxa, dwm = _rms_bwd(x1v, w_pre_mlp, acc)
        dx1v = dx2v + dxa
        dm, dwp = _rms_bwd(mixedv, w_post, dx1v)
        return dx1v, dm, jnp.sum(dwm, axis=0, keepdims=True), jnp.sum(dwp, axis=0, keepdims=True)

    dx1, dmixed, d_pre_mlp, d_post_mix = _matmul(
        dup, wup3, tb=True, b3=True, tm=512, tn=D, tk=DFF, out_dtypes=(F32, BF16, F32, F32),
        extra=(x1, dx2, mixed, pre_mlp_norm, post_mix_norm), epilogue=mid_bwd, n_sums=2, name="mm_dh2")
    dwout = _matmul(mix, dmixed, ta=True, tm=256, tn=1024, tk=2048, out_dtypes=(BF16,), name="mm_dwout")
    dmix = _matmul(dmixed, wout, tb=True, tm=512, tn=1024, tk=1024, name="mm_dmix")

    dfox, delta, d_fox_norm, from_sibling = _fox_norm_bwd(fox_o, dmix, w2, exchange=reduce_late[0](dwout, dwup3, dwdown))
    dproj, dcum_fox, reduced_a = _fox_bwd(proj, dfox, gates, lse, delta, exchange=reduce_late[1](from_sibling))
    (dproj, du, dw, dqd, dkd, da, dgl, d_gdn_norm), reduced_b = _gdn_scan_bwd(
        dmix, gdn_raw, proj, gdn_out_norm, u, w, qd, kd, a_intra, gl, states, dproj, exchange=reduce_late[2]())
    dqkv, dgates_gdn, reduced_c = _gdn_prep_bwd(qkv, gates, t_inv, du, dw, dqd, dkd, da, dgl, exchange=reduce_late[3]())
    reduced_late = (reduced_a, reduced_b, reduced_c)
    dproj, d_conv = _gdn_pre_bwd(proj, conv_w, dqkv, dproj)
    dproj, sums = _gates_bwd(proj, bias_vec, alog_vec, dgates_gdn, dcum_fox, dproj)

    dwin_p = _matmul(dproj, h, ta=True, tm=1280, tn=1024, tk=2048, out_dtypes=(BF16,), name="mm_dwin")
    exchange_in = reduce_in(dwin_p)
    dh = _matmul(dproj, win_p, tm=1024, tk=DPROJ_PAD, name="mm_dh", exchange=exchange_in)
    dh, reduced_in = dh if exchange_in is not None else (dh, [])
    grad_x, d_pre_mix = _pre_norm_bwd(dh, x, pre_mix_norm, dx1)

    d_norms = jnp.concatenate([d_pre_mix, d_post_mix, d_pre_mlp, d_post_mlp], axis=0)
    return grad_x, (d_norms, d_conv, sums, d_fox_norm, d_gdn_norm, loss_row), reduced_late, reduced_in


def kernel(x, pre_mix_norm, w_in, fox_f_bias, fox_out_norm, gdn_conv_w, gdn_a_log, gdn_dt_bias, gdn_out_norm, w_out, post_mix_norm, pre_mlp_norm, w_up, w_down, post_mlp_norm, loss_target, m_pre_mix_norm, m_w_in, m_fox_f_bias, m_fox_out_norm, m_gdn_conv_w, m_gdn_a_log, m_gdn_dt_bias, m_gdn_out_norm, m_w_out, m_post_mix_norm, m_pre_mlp_norm, m_w_up, m_w_down, m_post_mlp_norm, v_pre_mix_norm, v_w_in, v_fox_f_bias, v_fox_out_norm, v_gdn_conv_w, v_gdn_a_log, v_gdn_dt_bias, v_gdn_out_norm, v_w_out, v_post_mix_norm, v_pre_mlp_norm, v_w_up, v_w_down, v_post_mlp_norm):
    weights = dict(pre_mix_norm=pre_mix_norm, w_in=w_in, fox_f_bias=fox_f_bias, fox_out_norm=fox_out_norm, gdn_conv_w=gdn_conv_w,
                   gdn_a_log=gdn_a_log, gdn_dt_bias=gdn_dt_bias, gdn_out_norm=gdn_out_norm, w_out=w_out, post_mix_norm=post_mix_norm,
                   pre_mlp_norm=pre_mlp_norm, w_up=w_up, w_down=w_down, post_mlp_norm=post_mlp_norm)
    m_in = dict(pre_mix_norm=m_pre_mix_norm, w_in=m_w_in, fox_f_bias=m_fox_f_bias, fox_out_norm=m_fox_out_norm, gdn_conv_w=m_gdn_conv_w,
                gdn_a_log=m_gdn_a_log, gdn_dt_bias=m_gdn_dt_bias, gdn_out_norm=m_gdn_out_norm, w_out=m_w_out, post_mix_norm=m_post_mix_norm,
                pre_mlp_norm=m_pre_mlp_norm, w_up=m_w_up, w_down=m_w_down, post_mlp_norm=m_post_mlp_norm)
    v_in = dict(pre_mix_norm=v_pre_mix_norm, w_in=v_w_in, fox_f_bias=v_fox_f_bias, fox_out_norm=v_fox_out_norm, gdn_conv_w=v_gdn_conv_w,
                gdn_a_log=v_gdn_a_log, gdn_dt_bias=v_gdn_dt_bias, gdn_out_norm=v_gdn_out_norm, w_out=v_w_out, post_mix_norm=v_post_mix_norm,
                pre_mlp_norm=v_pre_mlp_norm, w_up=v_w_up, w_down=v_w_down, post_mlp_norm=v_post_mlp_norm)
    order_w = ("pre_mix_norm", "w_in", "fox_f_bias", "fox_out_norm", "gdn_conv_w", "gdn_a_log", "gdn_dt_bias", "gdn_out_norm", "w_out",
               "post_mix_norm", "pre_mlp_norm", "w_up", "w_down", "post_mlp_norm")
    big = ("w_in", "w_out", "w_up", "w_down")

    def row(v):
        return v if v.ndim == 2 else v.reshape(1, -1)

    win_shard = jnp.pad(w_in.T.astype(BF16), ((0, D - CW), (0, 0)))

    def resolve_first(gathered):
        win_g, conv_g = gathered
        return (_to_padded_rows(_with_own(win_g, win_shard)),
                _with_own(conv_g, gdn_conv_w).transpose(1, 0, 2).reshape(CONV_K, 3 * DGDN))

    late_shards = [weights[n].astype(BF16) for n in big[1:]]

    gathered_down = []

    def resolve_out_up(gathered_out, gathered_mlp):
        gathered_down.append(gathered_mlp[1])
        return _with_own(gathered_out[0], late_shards[0]).reshape(D, D), _with_own(gathered_mlp[0], late_shards[1])

    def resolve_down(_):
        return _with_own(gathered_down[0], late_shards[2]).reshape(DFF, D)

    pair_sums, late_blocks = {}, []

    def pair_summed(names, blocks, theirs):
        for n, s in zip(names, _pair_sum(blocks, theirs, "grads_pair_sum_" + names[0])):
            pair_sums[n] = s

    def late_pair_exchange(dwout, dwup3, dwdown):
        late_blocks.extend([dwout.reshape(NCHIP, D // NCHIP, D), dwup3, dwdown.reshape(NCHIP, DFF // NCHIP, D)])
        return _pair_exchange(late_blocks)

    def late_chip_exchange(theirs):
        pair_summed(big[1:], late_blocks, theirs)
        return _chip_exchange([pair_sums["w_up"], pair_sums["w_down"]])

    def reduce_in(dwin_p):
        pair_sums["w_in"] = _from_padded_rows_pair_sum(dwin_p)
        return _chip_exchange([pair_sums["w_in"]], rows_by_core=[(D // 2, CW_SENT - D // 2)])

    grad_x, small, received_late, received_in = _local_step(
        x[0], loss_target[0], (_allgather_exchange([win_shard], whole=[gdn_conv_w], sent_rows=[CW_SENT]), resolve_first),
        (_allgather_exchange(late_shards[:1]), _allgather_exchange(late_shards[1:]), None, resolve_out_up, resolve_down),
        (late_pair_exchange, late_chip_exchange, lambda: None, lambda: _chip_exchange([pair_sums["w_out"]])),
        reduce_in, row(pre_mix_norm), fox_f_bias, row(fox_out_norm), gdn_a_log, gdn_dt_bias,
        row(gdn_out_norm), row(post_mix_norm), row(pre_mlp_norm), row(post_mlp_norm))
    received_mlp, _, received_out = received_late

    g_mine, g_theirs, small_gathered = _chip_sum(
        [pair_sums[n] for n in big], list(received_in[:1]) + list(received_out[:1]) + list(received_mlp[:2]),
        exchange=_small_gather(*small))

    g_big, d_big, nm_big, nv_big = _adamw_big(
        [weights[n] for n in big[1:]], g_mine[1:], g_theirs[1:], [m_in[n] for n in big[1:]], [v_in[n] for n in big[1:]])
    in_t = _adamw_in(w_in.T, g_mine[0], g_theirs[0], m_w_in.T, v_w_in.T)
    g_small, d_small, nm_small, nv_small, loss_total = _small_adamw(
        small_gathered, [row(weights[n]) for n in SMALL_NAMES], [row(m_in[n]) for n in SMALL_NAMES],
        [row(v_in[n]) for n in SMALL_NAMES])

    grads, delta, new_m, new_v = {}, {}, {}, {}
    grads["w_in"], delta["w_in"], new_m["w_in"], new_v["w_in"] = [t.T for t in in_t]
    for i, n in enumerate(big[1:]):
        grads[n], delta[n], new_m[n], new_v[n] = g_big[i], d_big[i], nm_big[i], nv_big[i]
    for i, n in enumerate(SMALL_NAMES):
        shape = weights[n].shape
        grads[n], delta[n], new_m[n], new_v[n] = (g_small[i].reshape(shape), d_small[i].reshape(shape),
                                                  nm_small[i].reshape(shape), nv_small[i].reshape(shape))
    return (loss_total[0, 0], grad_x[None], *[grads[n] for n in order_w], *[delta[n] for n in order_w], *[new_m[n] for n in order_w],
            *[new_v[n] for n in order_w])
```

```python
import jax
import jax.numpy as jnp
from jax import lax
from jax.experimental import pallas as pl
from jax.experimental.pallas import tpu as pltpu

F32 = jnp.float32
BF16 = jnp.bfloat16
MESH = pl.DeviceIdType.MESH

S = 2048
D = 1024
NFH, FHD = 8, 64
NPAIR = NFH // 2
NGH, GHD = 4, 128
DFOX = NFH * FHD
DGDN = NGH * GHD
CHUNK = 64
NCH = S // CHUNK
CONV_K = 4
DFF = 4 * D
EPS = 1e-6
DPROJ = 3600
LANES = 128
DPROJ_PAD = 3840
BLK_GDN = 12
BLK_GZ = 24
BLK_SMALL = 28
NCHIP = 4
NDEV = 8
VMEM_LIMIT = 56 * 1024 * 1024

ADAM_LR = 0.001
ADAM_B1 = 0.9
ADAM_B2 = 0.999
ADAM_EPS = 1e-08
ADAM_WD = 0.01
ADAM_STEP = 10


def _cparams(**kw):
    return pltpu.CompilerParams(vmem_limit_bytes=VMEM_LIMIT, **kw)


def _dn(ca, cb):
    return (((ca,), (cb,)), ((), ()))


def _dot(a, b, ca=1, cb=0):
    return lax.dot_general(a.astype(BF16), b.astype(BF16), _dn(ca, cb), preferred_element_type=F32)


def _hdot(a, b, ca=1, cb=0):
    return lax.dot_general(a.astype(F32), b.astype(F32), _dn(ca, cb), precision=lax.Precision.HIGHEST,
                           preferred_element_type=F32)


def _dot3(a, b, ca=1, cb=0):
    a_hi, b_hi = a.astype(BF16), b.astype(BF16)
    a_lo, b_lo = (a - a_hi.astype(F32)).astype(BF16), (b - b_hi.astype(F32)).astype(BF16)
    dn = _dn(ca, cb)
    return (lax.dot_general(a_hi, b_hi, dn, preferred_element_type=F32)
            + (lax.dot_general(a_hi, b_lo, dn, preferred_element_type=F32)
               + lax.dot_general(a_lo, b_hi, dn, preferred_element_type=F32)))


@jax.custom_vjp
def _mm_nn(a, b):
    return _dot(a, b, 1, 0)


def _mm_nn_fwd(a, b):
    return _dot(a, b, 1, 0), (a, b)


def _mm_nn_bwd(res, g):
    a, b = res
    return _dot(g, b, 1, 1), _dot(a, g, 0, 0)


_mm_nn.defvjp(_mm_nn_fwd, _mm_nn_bwd)


@jax.custom_vjp
def _mm_nt(a, b):
    return _dot(a, b, 1, 1)


def _mm_nt_fwd(a, b):
    return _dot(a, b, 1, 1), (a, b)


def _mm_nt_bwd(res, g):
    a, b = res
    return _dot(g, b, 1, 0), _dot(g, a, 0, 0)


_mm_nt.defvjp(_mm_nt_fwd, _mm_nt_bwd)


@jax.custom_vjp
def _saved_inverse(m, t_inv):
    del m
    return t_inv


def _saved_inverse_fwd(m, t_inv):
    del m
    return t_inv, t_inv


def _saved_inverse_bwd(t_inv, g):
    return -_dot3(_dot3(t_inv, g, 0, 0), t_inv, 1, 1), jnp.zeros_like(t_inv)


_saved_inverse.defvjp(_saved_inverse_fwd, _saved_inverse_bwd)


def _sigmoid(z):
    return 1.0 / (1.0 + jnp.exp(-z))


def _softplus(z):
    return jnp.maximum(z, 0.0) + jnp.log(1.0 + jnp.exp(-jnp.abs(z)))


def _silu(z):
    return z * _sigmoid(z)


def _rms_scale(x):
    return lax.rsqrt(jnp.mean(x * x, axis=-1, keepdims=True) + EPS)


def _rms_bwd(x, w, g):
    r = _rms_scale(x)
    gw = g * w
    dx = r * gw - x * (r * r * r) * jnp.mean(gw * x, axis=-1, keepdims=True)
    return dx, g * x * r


def _matmul(a, b, *, name, ta=False, tb=False, tm=512, tn=512, tk=512, out_dtypes=(F32,), b3=False, o3=False,
            extra=(), epilogue=None, exchange=None, n_sums=0, lag=False):
    m, k = (a.shape[1], a.shape[0]) if ta else a.shape
    if b3:
        n = b.shape[1] if tb else b.shape[0] * b.shape[2]
        kb = b.shape[0] * b.shape[2] if tb else b.shape[1]
    else:
        n, kb = (b.shape[0], b.shape[1]) if tb else (b.shape[1], b.shape[0])
    assert kb == k, (name, kb, k)
    tm, tn, tk = min(tm, m), min(tn, n), min(tk, k)
    assert m % tm == 0 and n % tn == 0 and k % tk == 0, (name, m, n, k, tm, tn, tk)
    nk = k // tk
    whole_k_blocks = b3 and tb and not ta and nk == 1 and b.shape[0] > 1
    n_extra = len(extra)
    n_out = len(out_dtypes)
    row_tiles = m // tm
    assert not lag or (nk == 1 and tn == n and epilogue is not None and exchange is None), name
    grid = (row_tiles + (1 if lag else 0), n // tn, nk)
    ex_in, ex_in_specs, ex_out_specs, ex_out_shape, ex_scratch = _hosted(exchange)

    def body(*refs):
        a_ref, b_ref = refs[0], refs[1]
        extra_refs = refs[2:2 + n_extra]
        first_out = 2 + n_extra + len(ex_in)
        out_refs = refs[first_out:first_out + n_out]
        ex_refs = refs[2 + n_extra:first_out] + refs[first_out + n_out:first_out + n_out + len(ex_out_shape)] + refs[-2:]
        step = [pl.program_id(d) for d in range(3)]

        if exchange is not None:
            @pl.when((step[0] == 0) & (step[1] == 0) & (step[2] == 0))
            def _():
                exchange.start(*exchange.split(ex_refs))

        def finish(acc, row_tile):
            outs = (acc,) if epilogue is None else epilogue(acc, *[r[...] for r in extra_refs])
            for o_ref, val in zip(out_refs[:n_out - n_sums], outs):
                o_ref[...] = val.astype(o_ref.dtype)
            for o_ref, val in zip(out_refs[n_out - n_sums:], outs[n_out - n_sums:]):
                @pl.when(row_tile == 0)
                def _(o_ref=o_ref, val=val):
                    o_ref[...] = val

                @pl.when(row_tile > 0)
                def _(o_ref=o_ref, val=val):
                    o_ref[...] += val

        def product():
            if whole_k_blocks:
                width = b.shape[2]
                part = _dot(a_ref[:, 0:width], b_ref[0], 1, 1)
                for blk in range(1, b.shape[0]):
                    part = part + _dot(a_ref[:, blk * width:(blk + 1) * width], b_ref[blk], 1, 1)
                return part
            return _dot(a_ref[...], b_ref[...], 0 if ta else 1, 1 if tb else 0)

        scratch = first_out + n_out + len(ex_out_shape)
        if lag:
            accs = refs[scratch:scratch + 2]

            @pl.when(step[0] == 0)
            def _():
                accs[0][...] = product()

            for parity in (0, 1):
                @pl.when((step[0] > 0) & (step[0] < row_tiles) & (step[0] % 2 == parity))
                def _(parity=parity):
                    accs[parity][...] = product()
                    finish(accs[1 - parity][...], step[0] - 1)

            @pl.when(step[0] == row_tiles)
            def _():
                finish(accs[(row_tiles - 1) % 2][...], step[0] - 1)
        elif nk == 1:
            finish(product(), step[0])
        else:
            part = product()
            acc_ref = refs[scratch]

            @pl.when(step[2] == 0)
            def _():
                acc_ref[...] = part

            @pl.when(step[2] > 0)
            def _():
                acc_ref[...] += part

            @pl.when(step[2] == nk - 1)
            def _():
                finish(acc_ref[...], step[0])

        if exchange is not None:
            flat = (step[0] * grid[1] + step[1]) * nk + step[2]
            total = grid[0] * grid[1] * nk

            @pl.when(flat == total // 2)
            def _():
                exchange.middle(*exchange.split(ex_refs))

            @pl.when(flat == total - 1)
            def _():
                exchange.rest(*exchange.split(ex_refs))

    def fed(i):
        return jnp.minimum(i, row_tiles - 1) if lag else i

    def done(i):
        return jnp.maximum(i - 1, 0) if lag else i

    a_spec = (pl.BlockSpec((tk, tm), lambda i, j, kk: (kk, fed(i))) if ta
              else pl.BlockSpec((tm, tk), lambda i, j, kk: (fed(i), kk)))
    if whole_k_blocks:
        b_spec = pl.BlockSpec((b.shape[0], tn, b.shape[2]), lambda i, j, kk: (0, j, 0))
    elif b3 and tb:
        assert b.shape[2] == tk
        b_spec = pl.BlockSpec((None, tn, tk), lambda i, j, kk: (kk, j, 0))
    elif b3:
        assert b.shape[2] == tn
        b_spec = pl.BlockSpec((None, tk, tn), lambda i, j, kk: (j, kk, 0))
    elif tb:
        b_spec = pl.BlockSpec((tn, tk), lambda i, j, kk: (j, kk))
    else:
        b_spec = pl.BlockSpec((tk, tn), lambda i, j, kk: (kk, j))
    tile = pl.BlockSpec((tm, tn), lambda i, j, kk: (done(i), j))
    out_specs = [tile] * n_out
    out_shape = [jax.ShapeDtypeStruct((m, n), dt) for dt in out_dtypes]
    if o3:
        out_specs[0] = pl.BlockSpec((None, tm, tn), lambda i, j, kk: (j, done(i), 0))
        out_shape[0] = jax.ShapeDtypeStruct((n // tn, m, tn), out_dtypes[0])
    assert n_sums == 0 or tn == n
    for r in range(n_out - n_sums, n_out):
        out_specs[r] = pl.BlockSpec((1, tn), lambda i, j, kk: (0, 0))
        out_shape[r] = jax.ShapeDtypeStruct((1, n), out_dtypes[r])
    res = pl.pallas_call(
        body, name=name, grid=grid,
        in_specs=[a_spec, b_spec] + [tile if e.shape[0] == m else pl.BlockSpec((1, tn), lambda i, j, kk: (0, j)) for e in extra]
        + ex_in_specs, out_specs=out_specs + ex_out_specs,
        out_shape=out_shape + ex_out_shape,
        scratch_shapes=[pltpu.VMEM((tm, tn), F32)] * (2 if lag else 1 if nk > 1 else 0) + ex_scratch,
        compiler_params=_cparams(),
    )(a, b, *extra, *ex_in)
    if exchange is not None:
        return (res[0] if n_out == 1 else res[:n_out]), res[n_out:]
    return res[0] if n_out == 1 else res


TR = 512


def _row_spec(cols):
    return pl.BlockSpec((TR, cols), lambda i: (i, 0))


def _vec_spec(cols):
    return pl.BlockSpec((1, cols), lambda i: (0, 0))


def _pre_norm(x, w, exchange=None):
    ex_in, ex_in_specs, ex_out_specs, ex_out_shape, ex_scratch = _hosted(exchange)

    def body(*refs):
        x_ref, w_ref, h_ref = refs[0], refs[1], refs[2 + len(ex_in)]
        ex_refs = refs[2:2 + len(ex_in)] + refs[3 + len(ex_in):]
        if exchange is not None:
            @pl.when(pl.program_id(0) == 0)
            def _():
                exchange.start(*exchange.split(ex_refs))

        xv = x_ref[...]
        h_ref[...] = (xv * _rms_scale(xv) * w_ref[...]).astype(BF16)

        if exchange is not None:
            @pl.when(pl.program_id(0) == S // TR - 1)
            def _():
                exchange.finish(*exchange.split(ex_refs))

    res = pl.pallas_call(
        body, name="pre_norm", grid=(S // TR,), in_specs=[_row_spec(D), _vec_spec(D)] + ex_in_specs,
        out_specs=[_row_spec(D)] + ex_out_specs, out_shape=[jax.ShapeDtypeStruct((S, D), BF16)] + ex_out_shape,
        scratch_shapes=ex_scratch, compiler_params=_cparams(),
    )(x, w, *ex_in)
    return res[0], res[1:]


def _pre_norm_bwd(dh, x, w, dx1):
    def body(dh_ref, x_ref, w_ref, dx1_ref, dx_ref, dw_ref):
        i = pl.program_id(0)
        dxa, dwt = _rms_bwd(x_ref[...], w_ref[...], dh_ref[...])
        dx_ref[...] = dx1_ref[...] + dxa

        @pl.when(i == 0)
        def _():
            dw_ref[...] = jnp.zeros_like(dw_ref)

        dw_ref[...] += jnp.sum(dwt, axis=0, keepdims=True)

    return pl.pallas_call(
        body, name="pre_norm_bwd", grid=(S // TR,),
        in_specs=[_row_spec(D), _row_spec(D), _vec_spec(D), _row_spec(D)], out_specs=[_row_spec(D), _vec_spec(D)],
        out_shape=[jax.ShapeDtypeStruct((S, D), F32), jax.ShapeDtypeStruct((1, D), F32)], compiler_params=_cparams(),
    )(dh, x, w, dx1)


BQ = 512
NQ = S // BQ
LANE_BETA, LANE_G = 8, 12


def _gate_lanes(shape):
    lane = lax.broadcasted_iota(jnp.int32, shape, 1)
    return lane < LANE_BETA, (lane >= LANE_BETA) & (lane < LANE_G), (lane >= LANE_G) & (lane < LANE_G + NGH)


def _gates(proj, bias_vec, alog_vec):
    def body(s_ref, b_ref, a_ref, o_ref, carry_ref):
        i = pl.program_id(0)

        @pl.when(i == 0)
        def _():
            carry_ref[...] = jnp.zeros_like(carry_ref)

        z = s_ref[...] + b_ref[...]
        tail = jnp.log(1.0 + jnp.exp(-jnp.abs(z)))
        sp = jnp.maximum(z, 0.0) + tail
        lf = jnp.minimum(z, 0.0) - tail
        r = lax.broadcasted_iota(jnp.int32, (BQ, BQ), 0)
        c = lax.broadcasted_iota(jnp.int32, (BQ, BQ), 1)
        tri = (c <= r).astype(F32)
        cum = _hdot(tri, lf) + carry_ref[...]
        carry_ref[...] = cum[BQ - 1:BQ, :]
        is_fox, is_beta, is_g = _gate_lanes(z.shape)
        o_ref[...] = jnp.where(is_fox, cum, jnp.where(is_beta, _sigmoid(z), jnp.where(is_g, -jnp.exp(a_ref[...]) * sp, 0.0)))

    return pl.pallas_call(
        body, name="gates", grid=(NQ,),
        in_specs=[pl.BlockSpec((BQ, LANES), lambda i: (i, BLK_SMALL)), _vec_spec(LANES), _vec_spec(LANES)],
        out_specs=pl.BlockSpec((BQ, LANES), lambda i: (i, 0)), out_shape=jax.ShapeDtypeStruct((S, LANES), F32),
        scratch_shapes=[pltpu.VMEM((1, LANES), F32)], compiler_params=_cparams(),
    )(proj, bias_vec, alog_vec)


def _gates_bwd(proj, bias_vec, alog_vec, dgates_gdn, dcum_fox, dproj):
    def body(s_ref, b_ref, a_ref, dg_ref, dc_ref, dproj_in, dproj_ref, red_ref, carry_ref):
        del dproj_in
        i = pl.program_id(0)

        @pl.when(i == 0)
        def _():
            carry_ref[...] = jnp.zeros_like(carry_ref)
            red_ref[...] = jnp.zeros_like(red_ref)

        z = s_ref[...] + b_ref[...]
        dg = dg_ref[...] + dc_ref[...]
        r = lax.broadcasted_iota(jnp.int32, (BQ, BQ), 0)
        c = lax.broadcasted_iota(jnp.int32, (BQ, BQ), 1)
        upper = (c >= r).astype(F32)
        dlf = _hdot(upper, dg) + carry_ref[...]
        carry_ref[...] = dlf[0:1, :]
        sig = _sigmoid(z)
        g_scale = -jnp.exp(a_ref[...])
        is_fox, is_beta, is_g = _gate_lanes(z.shape)
        ds = jnp.where(is_fox, dlf * (1.0 - sig), jnp.where(is_beta, dg * sig * (1.0 - sig), jnp.where(is_g, dg * g_scale * sig, 0.0)))
        dproj_ref[:, 0:LANES] = ds.astype(BF16)
        dproj_ref[:, LANES:2 * LANES] = jnp.zeros((BQ, LANES), BF16)
        dalog = jnp.where(is_g, dg * g_scale * _softplus(z), 0.0)
        sums = jnp.sum(ds, axis=0, keepdims=True)
        red_ref[0:1, :] += jnp.where(is_fox[0:1], sums, 0.0)
        red_ref[1:2, :] += pltpu.roll(jnp.where(is_g[0:1], sums, 0.0), LANES - LANE_G, 1)
        red_ref[2:3, :] += pltpu.roll(jnp.sum(dalog, axis=0, keepdims=True), LANES - LANE_G, 1)

    blk = pl.BlockSpec((BQ, LANES), lambda i: (NQ - 1 - i, 0))
    return pl.pallas_call(
        body, name="gates_bwd", grid=(NQ,),
        in_specs=[pl.BlockSpec((BQ, LANES), lambda i: (NQ - 1 - i, BLK_SMALL)), _vec_spec(LANES), _vec_spec(LANES), blk, blk,
                  pl.BlockSpec(memory_space=pl.ANY)],
        out_specs=[pl.BlockSpec((BQ, 2 * LANES), lambda i: (NQ - 1 - i, BLK_SMALL // 2)), pl.BlockSpec((8, LANES), lambda i: (0, 0))],
        out_shape=[jax.ShapeDtypeStruct((S, DPROJ_PAD), BF16), jax.ShapeDtypeStruct((8, LANES), F32)],
        input_output_aliases={5: 0},
        scratch_shapes=[pltpu.VMEM((1, LANES), F32)], compiler_params=_cparams(),
    )(proj, bias_vec, alog_vec, dgates_gdn, dcum_fox, dproj)


FOX_SCALE = FHD ** -0.5
FOX_PAIRS = 2
FOX_PAIRS_BWD = 2


def _head_mask(e):
    lane = lax.broadcasted_iota(jnp.int32, (1, LANES), 1)
    return (lane >= e * FHD) & (lane < (e + 1) * FHD)


def _lane_col(vals, index):
    lane = lax.broadcasted_iota(jnp.int32, vals.shape, 1)
    return jnp.sum(jnp.where(lane == index, vals, 0.0), axis=1, keepdims=True)


def _sublane_row(vals, index):
    row = lax.broadcasted_iota(jnp.int32, vals.shape, 0)
    return jnp.sum(jnp.where(row == index, vals, 0.0), axis=0, keepdims=True)


def _pair_cols(c0, c1):
    lane = lax.broadcasted_iota(jnp.int32, (c0.shape[0], 2), 1)
    return jnp.where(lane == 0, c0, c1)


def _split3(x):
    hi = x.astype(BF16).astype(F32)
    rest = x - hi
    mid = rest.astype(BF16).astype(F32)
    return hi, mid, (rest - mid).astype(BF16).astype(F32)


def _fox_operand(vals, e, cum, is_query):
    lane = lax.broadcasted_iota(jnp.int32, (1, LANES), 1)
    base = (1 - e) * FHD
    parts = _split3(cum)
    own = jnp.where(_head_mask(e), vals * FOX_SCALE if is_query else vals, 0.0)
    cum_at, ones_at = (base, base + 3) if is_query else (base + 3, base)
    sign = 1.0 if is_query else -1.0
    out = own + jnp.where((lane >= ones_at) & (lane < ones_at + 3), 1.0, 0.0)
    for i, part in enumerate(parts):
        out = out + jnp.where(lane == cum_at + i, sign * part, 0.0)
    return out.astype(BF16)


def _causal_block():
    return lax.broadcasted_iota(jnp.int32, (BQ, BQ), 1) <= lax.broadcasted_iota(jnp.int32, (BQ, BQ), 0)


def _head_rms(o, masks):
    o2 = o * o
    r = [lax.rsqrt(jnp.sum(jnp.where(mk, o2, 0.0), axis=1, keepdims=True) * (1.0 / FHD) + EPS) for mk in masks]
    return jnp.where(masks[0], r[0], r[1])


def _hosted(exchange):
    if exchange is None:
        return [], [], [], [], []
    return (exchange.inputs, [HBM] * len(exchange.inputs), [HBM] * len(exchange.out_shape), exchange.out_shape,
            exchange.sem_shapes())


def _fox_fwd(proj, gates, w2, exchange=None):
    ex_in, ex_in_specs, ex_out_specs, ex_out_shape, ex_scratch = _hosted(exchange)

    n_in = 3 * FOX_PAIRS + 2
    heads = [(pp, e) for pp in range(FOX_PAIRS) for e in range(2)]

    def body(*refs):
        qkv_refs, g_ref, w_ref = refs[:3 * FOX_PAIRS], refs[3 * FOX_PAIRS], refs[3 * FOX_PAIRS + 1]
        mix_ref, o_ref, lse_ref = refs[n_in + len(ex_in):n_in + 3 + len(ex_in)]
        ka_ref, vb_ref = refs[n_in + 3 + len(ex_in) + len(ex_out_shape):n_in + 5 + len(ex_in) + len(ex_out_shape)]
        ex_refs = refs[n_in:n_in + len(ex_in)] + refs[n_in + 3 + len(ex_in):n_in + 3 + len(ex_in) + len(ex_out_shape)] + refs[-2:]
        grp, qi = pl.program_id(0), pl.program_id(1)

        def head_index(pp, e):
            return 2 * (FOX_PAIRS * grp + pp) + e

        if exchange is not None:
            @pl.when((grp == 0) & (qi == 0))
            def _():
                exchange.start(*exchange.split(ex_refs))

        @pl.when(qi == 0)
        def _():
            gt = g_ref[...]
            for pp in range(FOX_PAIRS):
                kv = qkv_refs[3 * pp + 1][...]
                for e in range(2):
                    ka_ref[2 * pp + e] = _fox_operand(kv, e, _lane_col(gt, head_index(pp, e)), False)
                vb_ref[pp] = qkv_refs[3 * pp + 2][...].astype(BF16)

        masks = [_head_mask(0), _head_mask(1)]
        gt = g_ref[pl.ds(pl.multiple_of(qi * BQ, BQ), BQ), :]
        qs = [_fox_operand(qkv_refs[3 * pp][...], e, _lane_col(gt, head_index(pp, e)), True) for pp, e in heads]
        n = range(len(heads))

        def block(kj, carry, diagonal):
            rows = pl.ds(pl.multiple_of(kj * BQ, BQ), BQ)
            s = [_dot(qs[i], ka_ref[i, rows, :], 1, 1) for i in n]
            if diagonal:
                s = [jnp.where(_causal_block(), s[i], -jnp.inf) for i in n]
            m_new = [jnp.maximum(carry[i][0], jnp.max(s[i], axis=-1, keepdims=True)) for i in n]
            p = [jnp.exp(s[i] - m_new[i]) for i in n]
            alpha = [jnp.exp(carry[i][0] - m_new[i]) for i in n]
            l_new = [alpha[i] * carry[i][1] + jnp.sum(p[i], axis=-1, keepdims=True) for i in n]
            pv = [_dot(p[i], vb_ref[heads[i][0], rows, :]) for i in n]
            return tuple((m_new[i], l_new[i], alpha[i] * carry[i][2] + pv[i]) for i in n)

        one = (jnp.full((BQ, 1), -jnp.inf, F32), jnp.zeros((BQ, 1), F32), jnp.zeros((BQ, LANES), F32))
        below = lax.fori_loop(0, qi, lambda kj, carry: block(kj, carry, False), (one,) * len(heads))
        done = block(qi, below, True)
        for pp in range(FOX_PAIRS):
            (m0, l0, a0), (m1, l1, a1) = done[2 * pp], done[2 * pp + 1]
            o = jnp.where(masks[0], a0 / l0, a1 / l1)
            cols = slice(pp * LANES, (pp + 1) * LANES)
            o_ref[:, cols] = o
            mix_ref[:, cols] = (o * _head_rms(o, masks) * w_ref[...]).astype(BF16)
            lse_ref[pp] = _pair_cols(m0 + jnp.log(l0), m1 + jnp.log(l1))

        if exchange is not None:
            @pl.when((grp == NPAIR // FOX_PAIRS // 2) & (qi == 0))
            def _():
                exchange.middle(*exchange.split(ex_refs))

            @pl.when((grp == NPAIR // FOX_PAIRS - 1) & (qi == NQ - 1))
            def _():
                exchange.rest(*exchange.split(ex_refs))

    qkv_specs = []
    for pp in range(FOX_PAIRS):
        qkv_specs.append(pl.BlockSpec((BQ, LANES), lambda g, i, pp=pp: (i, 3 * (FOX_PAIRS * g + pp))))
        qkv_specs.append(pl.BlockSpec((S, LANES), lambda g, i, pp=pp: (0, 3 * (FOX_PAIRS * g + pp) + 1)))
        qkv_specs.append(pl.BlockSpec((S, LANES), lambda g, i, pp=pp: (0, 3 * (FOX_PAIRS * g + pp) + 2)))
    blk = pl.BlockSpec((BQ, FOX_PAIRS * LANES), lambda g, i: (i, g))
    res = pl.pallas_call(
        body, name="fox_fwd", grid=(NPAIR // FOX_PAIRS, NQ),
        in_specs=qkv_specs + [pl.BlockSpec((S, LANES), lambda g, i: (0, 0)), pl.BlockSpec((1, LANES), lambda g, i: (0, 0))]
        + ex_in_specs,
        out_specs=[blk, blk, pl.BlockSpec((FOX_PAIRS, BQ, 2), lambda g, i: (g, i, 0))] + ex_out_specs,
        out_shape=[jax.ShapeDtypeStruct((S, D), BF16), jax.ShapeDtypeStruct((S, DFOX), F32),
                   jax.ShapeDtypeStruct((NPAIR, S, 2), F32)] + ex_out_shape,
        scratch_shapes=[pltpu.VMEM((2 * FOX_PAIRS, S, LANES), BF16), pltpu.VMEM((FOX_PAIRS, S, LANES), BF16)] + ex_scratch,
        compiler_params=_cparams(),
    )(*([proj] * (3 * FOX_PAIRS)), gates, w2, *ex_in)
    return res[0], res[1], res[2], res[3:]


def _fox_norm_bwd(o, dmix, w2, exchange=None):
    ex_in, ex_in_specs, ex_out_specs, ex_out_shape, ex_scratch = _hosted(exchange)

    def body(*refs):
        o_ref, g_ref, w_ref = refs[:3]
        do_ref, dl_ref, dw_ref = refs[3 + len(ex_in):6 + len(ex_in)]
        ex_refs = refs[3:3 + len(ex_in)] + refs[6 + len(ex_in):]
        hp, qi = pl.program_id(0), pl.program_id(1)

        if exchange is not None:
            @pl.when((hp == 0) & (qi == 0))
            def _():
                exchange.start(*exchange.split(ex_refs))

        masks = [_head_mask(0), _head_mask(1)]
        ov = o_ref[...]
        g = g_ref[...]
        r = _head_rms(ov, masks)
        gw = g * w_ref[...]
        gwo = gw * ov
        mean = [jnp.sum(jnp.where(mk, gwo, 0.0), axis=1, keepdims=True) * (1.0 / FHD) for mk in masks]
        do = r * gw - ov * (r * r * r) * jnp.where(masks[0], mean[0], mean[1])
        do_ref[...] = do.astype(BF16)
        doo = do * ov
        dl_ref[...] = _pair_cols(*[jnp.sum(jnp.where(mk, doo, 0.0), axis=1, keepdims=True) for mk in masks])

        @pl.when((hp == 0) & (qi == 0))
        def _():
            dw_ref[...] = jnp.zeros_like(dw_ref)

        dw_ref[...] += jnp.sum(g * ov * r, axis=0, keepdims=True)

        @pl.when((hp == NPAIR - 1) & (qi == NQ - 1))
        def _():
            dw = dw_ref[...]
            dw_ref[...] = dw + pltpu.roll(dw, FHD, 1)
            if exchange is not None:
                exchange.finish(*exchange.split(ex_refs))

    blk = pl.BlockSpec((BQ, LANES), lambda hp, i: (i, hp))
    vec = pl.BlockSpec((1, LANES), lambda hp, i: (0, 0))
    res = pl.pallas_call(
        body, name="fox_norm_bwd", grid=(NPAIR, NQ), in_specs=[blk, blk, vec] + ex_in_specs,
        out_specs=[blk, pl.BlockSpec((None, BQ, 2), lambda hp, i: (hp, i, 0)), vec] + ex_out_specs,
        out_shape=[jax.ShapeDtypeStruct((S, DFOX), BF16), jax.ShapeDtypeStruct((NPAIR, S, 2), F32),
                   jax.ShapeDtypeStruct((1, LANES), F32)] + ex_out_shape,
        scratch_shapes=ex_scratch, compiler_params=_cparams(),
    )(o, dmix, w2, *ex_in)
    return res[0], res[1], res[2], res[3:]


def _fox_bwd(proj, do, gates, lse, delta, exchange=None):
    ex_in, ex_in_specs, ex_out_specs, ex_out_shape, ex_scratch = _hosted(exchange)

    pg = FOX_PAIRS_BWD
    n_in = 3 * pg + 4
    heads = [(pp, e) for pp in range(pg) for e in range(2)]

    def body(*refs):
        qkv_refs = refs[:3 * pg]
        do_ref, g_ref, lse_ref, dl_ref = refs[3 * pg:n_in]
        dproj_ref, dc_ref = refs[n_in + len(ex_in):n_in + 2 + len(ex_in)]
        qa_ref, dq_ref = refs[n_in + 2 + len(ex_in) + len(ex_out_shape):n_in + 4 + len(ex_in) + len(ex_out_shape)]
        ex_refs = refs[n_in:n_in + len(ex_in)] + refs[n_in + 2 + len(ex_in):n_in + 2 + len(ex_in) + len(ex_out_shape)] + refs[-2:]
        grp, kj = pl.program_id(0), pl.program_id(1)

        def head_index(pp, e):
            return 2 * (pg * grp + pp) + e

        if exchange is not None:
            @pl.when((grp == 0) & (kj == 0))
            def _():
                exchange.start(*exchange.split(ex_refs))

        @pl.when(kj == 0)
        def _():
            gt = g_ref[...]
            for pp in range(pg):
                qv = qkv_refs[3 * pp][...]
                for e in range(2):
                    qa_ref[2 * pp + e] = _fox_operand(qv, e, _lane_col(gt, head_index(pp, e)), True)
            dq_ref[...] = jnp.zeros_like(dq_ref)

        @pl.when((grp == 0) & (kj == 0))
        def _():
            dc_ref[...] = jnp.zeros_like(dc_ref)

        masks = [_head_mask(0), _head_mask(1)]
        krows = pl.ds(pl.multiple_of(kj * BQ, BQ), BQ)
        gk = g_ref[krows, :]
        kas = [_fox_operand(qkv_refs[3 * pp + 1][...], e, _lane_col(gk, head_index(pp, e)), False) for pp, e in heads]
        vbs = [qkv_refs[3 * pp + 2][...].astype(BF16) for pp in range(pg)]
        lane = lax.broadcasted_iota(jnp.int32, (BQ, LANES), 1)
        n = range(len(heads))

        def block(qi, carry, diagonal):
            dks, dvs, css = carry
            rows = pl.ds(pl.multiple_of(qi * BQ, BQ), BQ)
            qa = [qa_ref[i, rows, :] for i in n]
            s = [_dot(qa[i], kas[i], 1, 1) for i in n]
            if diagonal:
                s = [jnp.where(_causal_block(), s[i], -jnp.inf) for i in n]
            dov = [do_ref[rows, pp * LANES:(pp + 1) * LANES] for pp in range(pg)]
            doe = [jnp.where(masks[e], dov[pp], jnp.zeros_like(dov[pp])) for pp, e in heads]
            lse2 = [lse_ref[pp, rows, :] for pp in range(pg)]
            dl2 = [dl_ref[pp, rows, :] for pp in range(pg)]
            p = [jnp.exp(s[i] - _lane_col(lse2[heads[i][0]], heads[i][1])) for i in n]
            dp = [_dot(doe[i], vbs[heads[i][0]], 1, 1) for i in n]
            ds = [p[i] * (dp[i] - _lane_col(dl2[heads[i][0]], heads[i][1])) for i in n]
            dv_part = [_dot(p[i], doe[i], 0, 0) for i in n]
            dk_part = [_dot(ds[i], jnp.where(masks[heads[i][1]], qa[i], jnp.zeros_like(qa[i])), 0, 0) for i in n]
            dq_part = [jnp.where(masks[heads[i][1]], _dot(ds[i], kas[i]), 0.0) for i in n]
            css = tuple(css[i] + jnp.sum(ds[i], axis=0, keepdims=True) for i in n)
            dc = jnp.zeros((BQ, LANES), F32)
            for i in n:
                dc = dc + jnp.where(lane == head_index(*heads[i]), jnp.sum(ds[i], axis=1, keepdims=True), 0.0)
            for pp in range(pg):
                dq_ref[pp, rows, :] += (dq_part[2 * pp] + dq_part[2 * pp + 1]) * FOX_SCALE
            dc_ref[rows, :] += dc
            dks = tuple(dks[pp] + dk_part[2 * pp] + dk_part[2 * pp + 1] for pp in range(pg))
            dvs = tuple(dvs[pp] + dv_part[2 * pp] + dv_part[2 * pp + 1] for pp in range(pg))
            return dks, dvs, css

        zero = jnp.zeros((BQ, LANES), F32)
        first = block(kj, ((zero,) * pg, (zero,) * pg, (jnp.zeros((1, BQ), F32),) * len(heads)), True)
        dks, dvs, css = lax.fori_loop(kj + 1, NQ, lambda qi, carry: block(qi, carry, False), first)
        r = lax.broadcasted_iota(jnp.int32, (BQ, BQ), 0)
        c = lax.broadcasted_iota(jnp.int32, (BQ, BQ), 1)
        dcol = jnp.zeros((BQ, LANES), F32)
        for i in n:
            col = jnp.sum(jnp.where(r == c, css[i], 0.0), axis=1, keepdims=True)
            dcol = dcol + jnp.where(lane == head_index(*heads[i]), col, 0.0)
        dc_ref[krows, :] -= dcol
        for pp in range(pg):
            base = 3 * pp * LANES
            dproj_ref[krows, base + LANES:base + 2 * LANES] = dks[pp].astype(BF16)
            dproj_ref[krows, base + 2 * LANES:base + 3 * LANES] = dvs[pp].astype(BF16)

        @pl.when(kj == NQ - 1)
        def _():
            for pp in range(pg):
                dproj_ref[:, 3 * pp * LANES:(3 * pp + 1) * LANES] = dq_ref[pp].astype(BF16)

        if exchange is not None:
            @pl.when((grp == NPAIR // pg // 2) & (kj == 0))
            def _():
                exchange.middle(*exchange.split(ex_refs))

            @pl.when((grp == NPAIR // pg - 1) & (kj == NQ - 1))
            def _():
                exchange.rest(*exchange.split(ex_refs))

    qkv_specs = []
    for pp in range(pg):
        qkv_specs.append(pl.BlockSpec((S, LANES), lambda g, j, pp=pp: (0, 3 * (pg * g + pp))))
        qkv_specs.append(pl.BlockSpec((BQ, LANES), lambda g, j, pp=pp: (j, 3 * (pg * g + pp) + 1)))
        qkv_specs.append(pl.BlockSpec((BQ, LANES), lambda g, j, pp=pp: (j, 3 * (pg * g + pp) + 2)))
    pair = pl.BlockSpec((pg, S, 2), lambda g, j: (g, 0, 0))
    res = pl.pallas_call(
        body, name="fox_bwd", grid=(NPAIR // pg, NQ),
        in_specs=qkv_specs + [pl.BlockSpec((S, pg * LANES), lambda g, j: (0, g)), pl.BlockSpec((S, LANES), lambda g, j: (0, 0)),
                              pair, pair] + ex_in_specs,
        out_specs=[pl.BlockSpec((S, 3 * pg * LANES), lambda g, j: (0, g)), pl.BlockSpec((S, LANES), lambda g, j: (0, 0))]
        + ex_out_specs,
        out_shape=[jax.ShapeDtypeStruct((S, DPROJ_PAD), BF16), jax.ShapeDtypeStruct((S, LANES), F32)] + ex_out_shape,
        scratch_shapes=[pltpu.VMEM((2 * pg, S, LANES), BF16), pltpu.VMEM((pg, S, LANES), F32)] + ex_scratch,
        compiler_params=_cparams(),
    )(*([proj] * (3 * pg)), do, gates, lse, delta, *ex_in)
    return res[0], res[1], res[2:]


NQKV = 3 * NGH
GDN_QSCALE = GHD ** -0.5


def _shift_down(x, s):
    if s == 0:
        return x
    row = lax.broadcasted_iota(jnp.int32, x.shape, 0)
    return jnp.where(row >= s, pltpu.roll(x, s, 0), 0.0)


def _shift_up(x, s):
    if s == 0:
        return x
    n = x.shape[0]
    row = lax.broadcasted_iota(jnp.int32, x.shape, 0)
    return jnp.where(row < n - s, pltpu.roll(x, n - s, 0), 0.0)


def _conv_taps(xv):
    return [_shift_down(xv, CONV_K - 1 - j) for j in range(CONV_K)]


def _conv_pre(taps, wv):
    pre = taps[CONV_K - 1] * wv[CONV_K - 1:CONV_K, :]
    for j in range(CONV_K - 1):
        pre = pre + taps[j] * wv[j:j + 1, :]
    return pre


def _l2_factors(b):
    return b < 2 * NGH, jnp.where(b < NGH, GDN_QSCALE, 1.0)


def _gdn_pre(proj, conv_w):
    def body(x_ref, w_ref, o_ref):
        b = pl.program_id(0)
        c = _silu(_conv_pre(_conv_taps(x_ref[...]), w_ref[...]))
        normed, scale = _l2_factors(b)
        rs = lax.rsqrt(jnp.sum(c * c, axis=-1, keepdims=True) + EPS)
        o_ref[...] = c * jnp.where(normed, rs, 1.0) * scale

    return pl.pallas_call(
        body, name="gdn_pre", grid=(NQKV,),
        in_specs=[pl.BlockSpec((S, GHD), lambda b: (0, BLK_GDN + b)), pl.BlockSpec((CONV_K, GHD), lambda b: (0, b))],
        out_specs=pl.BlockSpec((S, GHD), lambda b: (0, b)),
        out_shape=jax.ShapeDtypeStruct((S, NQKV * GHD), F32), compiler_params=_cparams(),
    )(proj, conv_w)


def _gdn_pre_bwd(proj, conv_w, dqkv, dproj):
    def body(x_ref, w_ref, dy_ref, dproj_in, dx_ref, dw_ref):
        del dproj_in
        b = pl.program_id(0)
        taps = _conv_taps(x_ref[...])
        wv = w_ref[...]
        pre = _conv_pre(taps, wv)
        sig = _sigmoid(pre)
        c = pre * sig
        normed, scale = _l2_factors(b)
        g = dy_ref[...] * scale
        rs = lax.rsqrt(jnp.sum(c * c, axis=-1, keepdims=True) + EPS)
        dc_n = rs * g - c * (rs * rs * rs) * jnp.sum(g * c, axis=-1, keepdims=True)
        dc = jnp.where(normed, dc_n, g)
        dpre = dc * sig * (1.0 + pre * (1.0 - sig))
        dx = dpre * wv[CONV_K - 1:CONV_K, :]
        for j in range(CONV_K - 1):
            dx = dx + _shift_up(dpre, CONV_K - 1 - j) * wv[j:j + 1, :]
        dx_ref[...] = dx.astype(BF16)
        for j in range(CONV_K):
            dw_ref[j:j + 1, :] = jnp.sum(dpre * taps[j], axis=0, keepdims=True)

    return pl.pallas_call(
        body, name="gdn_pre_bwd", grid=(NQKV,),
        in_specs=[pl.BlockSpec((S, GHD), lambda b: (0, BLK_GDN + b)), pl.BlockSpec((CONV_K, GHD), lambda b: (0, b)),
                  pl.BlockSpec((None, S, GHD), lambda b: (b // NGH, 0, b % NGH)), pl.BlockSpec(memory_space=pl.ANY)],
        out_specs=[pl.BlockSpec((S, GHD), lambda b: (0, BLK_GDN + b)), pl.BlockSpec((CONV_K, GHD), lambda b: (0, b))],
        out_shape=[jax.ShapeDtypeStruct((S, DPROJ_PAD), BF16), jax.ShapeDtypeStruct((CONV_K, NQKV * GHD), F32)],
        input_output_aliases={3: 0}, compiler_params=_cparams(),
    )(proj, conv_w, dqkv, dproj)


CB = 16
NCB = NCH // CB


def _chunk_prep(qs, ks, vs, gcols, bcols, t_saved=None):
    n = range(len(qs))
    r = lax.broadcasted_iota(jnp.int32, (CHUNK, CHUNK), 0)
    c = lax.broadcasted_iota(jnp.int32, (CHUNK, CHUNK), 1)
    incl = c <= r
    eye = (r == c).astype(F32)
    grow = [jnp.sum(gcols[i] * eye, axis=0, keepdims=True) for i in n]
    gc_col = [jnp.sum(jnp.where(incl, grow[i], 0.0), axis=1, keepdims=True) for i in n]
    gc_row = [jnp.sum(jnp.where(r <= c, gcols[i], 0.0), axis=0, keepdims=True) for i in n]
    decay = [jnp.exp(jnp.where(incl, gc_col[i] - gc_row[i], -jnp.inf)) for i in n]
    kb = [ks[i] * bcols[i] for i in n]
    vb = [vs[i] * bcols[i] for i in n]
    kk = [_mm_nt(kb[i], ks[i]) for i in n]
    m = [jnp.where(c < r, kk[i] * decay[i], 0.0) for i in n]
    if t_saved is None:
        t_inv = [eye - m[i] for i in n]
        p = [_dot3(m[i], m[i]) for i in n]
        for step in range(5):
            t_inv = [t_inv[i] + _dot3(t_inv[i], p[i]) for i in n]
            if step < 4:
                p = [_dot3(p[i], p[i]) for i in n]
    else:
        t_inv = [_saved_inverse(m[i], t_saved[i]) for i in n]
    egc = [jnp.exp(gc_col[i]) for i in n]
    u = [_mm_nn(t_inv[i], vb[i]) for i in n]
    w = [_mm_nn(t_inv[i], kb[i] * egc[i]) for i in n]
    qk = [_mm_nt(qs[i], ks[i]) for i in n]
    gc_last = [gc_col[i][CHUNK - 1:CHUNK, :] for i in n]
    return [(u[i], w[i], qk[i] * decay[i], qs[i] * egc[i], ks[i] * jnp.exp(gc_last[i] - gc_col[i]), jnp.exp(gc_last[i]),
             t_inv[i]) for i in n]


def _prep_specs():
    rows = CB * CHUNK
    qs = pl.BlockSpec((rows, GHD), lambda i, h: (i, h))
    ks = pl.BlockSpec((rows, GHD), lambda i, h: (i, NGH + h))
    vs = pl.BlockSpec((rows, GHD), lambda i, h: (i, 2 * NGH + h))
    gs = pl.BlockSpec((rows, LANES), lambda i, h: (i, 0))
    a_s = pl.BlockSpec((None, rows, CHUNK), lambda i, h: (h, i, 0))
    gl_s = pl.BlockSpec((None, CB, 1, LANES), lambda i, h: (h, i, 0, 0))
    return qs, ks, vs, gs, a_s, gl_s


def _gdn_prep(qkv, gates, exchange=None):
    ex_in, ex_in_specs, ex_out_specs, ex_out_shape, ex_scratch = _hosted(exchange)

    def body(*refs):
        q_ref, k_ref, v_ref, g_ref = refs[:4]
        u_ref, w_ref, qd_ref, kd_ref, a_ref, gl_ref, t_ref = refs[4 + len(ex_in):11 + len(ex_in)]
        ex_refs = refs[4:4 + len(ex_in)] + refs[11 + len(ex_in):]
        h = pl.program_id(1)

        if exchange is not None:
            @pl.when((pl.program_id(0) == 0) & (h == 0))
            def _():
                exchange.start(*exchange.split(ex_refs))

        chunks = [pl.ds(cidx * CHUNK, CHUNK) for cidx in range(CB)]
        gts = [g_ref[rows, :] for rows in chunks]
        outs = _chunk_prep([q_ref[rows, :] for rows in chunks], [k_ref[rows, :] for rows in chunks],
                           [v_ref[rows, :] for rows in chunks], [_lane_col(gt, LANE_G + h) for gt in gts],
                           [_lane_col(gt, LANE_BETA + h) for gt in gts])
        for cidx, rows in enumerate(chunks):
            u, w, a, qd, kd, gl, t_inv = outs[cidx]
            u_ref[rows, :] = u
            w_ref[rows, :] = w
            qd_ref[rows, :] = qd
            kd_ref[rows, :] = kd
            a_ref[rows, :] = a
            t_ref[rows, :] = t_inv
            gl_ref[cidx] = jnp.broadcast_to(gl, (1, LANES))

        if exchange is not None:
            @pl.when((pl.program_id(0) == NCB // 2) & (h == 0))
            def _():
                exchange.middle(*exchange.split(ex_refs))

            @pl.when((pl.program_id(0) == NCB - 1) & (h == NGH - 1))
            def _():
                exchange.rest(*exchange.split(ex_refs))

    qs, ks, vs, gs, a_s, gl_s = _prep_specs()
    tok = jax.ShapeDtypeStruct((S, DGDN), F32)
    sq = jax.ShapeDtypeStruct((NGH, S, CHUNK), F32)
    res = pl.pallas_call(
        body, name="gdn_prep", grid=(NCB, NGH), in_specs=[qs, ks, vs, gs] + ex_in_specs,
        out_specs=[qs, qs, qs, qs, a_s, gl_s, a_s] + ex_out_specs,
        out_shape=[tok, tok, tok, tok, sq, jax.ShapeDtypeStruct((NGH, NCH, 1, LANES), F32), sq] + ex_out_shape,
        scratch_shapes=ex_scratch, compiler_params=_cparams(),
    )(qkv, qkv, qkv, gates, *ex_in)
    return res[:7], res[7:]


def _gdn_prep_bwd(qkv, gates, t_inv, du, dw, dqd, dkd, da, dgl, exchange=None):
    ex_in, ex_in_specs, ex_out_specs, ex_out_shape, ex_scratch = _hosted(exchange)

    def body(*refs):
        q_ref, k_ref, v_ref, g_ref, t_ref, du_ref, dw_ref, dqd_ref, dkd_ref, da_ref, dgl_ref = refs[:11]
        dqkv_ref, dg_ref = refs[11 + len(ex_in):13 + len(ex_in)]
        ex_refs = refs[11:11 + len(ex_in)] + refs[13 + len(ex_in):]
        h = pl.program_id(1)

        if exchange is not None:
            @pl.when((pl.program_id(0) == 0) & (h == 0))
            def _():
                exchange.start(*exchange.split(ex_refs))

        @pl.when(h == 0)
        def _():
            dg_ref[...] = jnp.zeros_like(dg_ref)

        lane = lax.broadcasted_iota(jnp.int32, (CHUNK, LANES), 1)
        chunks = [pl.ds(cidx * CHUNK, CHUNK) for cidx in range(CB)]
        gts = [g_ref[rows, :] for rows in chunks]
        t_saved = [t_ref[rows, :] for rows in chunks]
        _, vjp = jax.vjp(lambda *args: [o[:6] for o in _chunk_prep(*args, t_saved=t_saved)],
                         [q_ref[rows, :] for rows in chunks], [k_ref[rows, :] for rows in chunks],
                         [v_ref[rows, :] for rows in chunks], [_lane_col(gt, LANE_G + h) for gt in gts],
                         [_lane_col(gt, LANE_BETA + h) for gt in gts])
        dqs, dks, dvs, dgcs, dbcs = vjp([(du_ref[rows, :], dw_ref[rows, :], da_ref[rows, :], dqd_ref[rows, :],
                                          dkd_ref[rows, :], dgl_ref[cidx][:, 0:1]) for cidx, rows in enumerate(chunks)])
        for cidx, rows in enumerate(chunks):
            dq, dk, dv, dgc, dbc = dqs[cidx], dks[cidx], dvs[cidx], dgcs[cidx], dbcs[cidx]
            dqkv_ref[0, rows, :] = dq
            dqkv_ref[1, rows, :] = dk
            dqkv_ref[2, rows, :] = dv
            dg_ref[rows, :] += jnp.where(lane == LANE_G + h, dgc, 0.0) + jnp.where(lane == LANE_BETA + h, dbc, 0.0)

        if exchange is not None:
            @pl.when((pl.program_id(0) == NCB - 1) & (h == NGH - 1))
            def _():
                exchange.finish(*exchange.split(ex_refs))

    qs, ks, vs, gs, a_s, gl_s = _prep_specs()
    res = pl.pallas_call(
        body, name="gdn_prep_bwd", grid=(NCB, NGH), in_specs=[qs, ks, vs, gs, a_s, qs, qs, qs, qs, a_s, gl_s] + ex_in_specs,
        out_specs=[pl.BlockSpec((3, CB * CHUNK, GHD), lambda i, h: (0, i, h)), gs] + ex_out_specs,
        out_shape=[jax.ShapeDtypeStruct((3, S, DGDN), F32), jax.ShapeDtypeStruct((S, LANES), F32)] + ex_out_shape,
        scratch_shapes=ex_scratch, compiler_params=_cparams(),
    )(qkv, qkv, qkv, gates, t_inv, du, dw, dqd, dkd, da, dgl, *ex_in)
    return res[0], res[1], res[2:]


def _scan_specs(nh, parts, reverse):
    wide, rows, chunks = nh * GHD, S // parts, NCH // parts

    def part(p):
        return parts - 1 - p if reverse else p

    hs = pl.BlockSpec((rows, wide), lambda g, p: (part(p), g))
    a_s = pl.BlockSpec((nh, rows, CHUNK), lambda g, p: (g, part(p), 0))
    gl_s = pl.BlockSpec((nh, chunks, 1, LANES), lambda g, p: (g, part(p), 0, 0))
    st_s = pl.BlockSpec((nh, chunks, GHD, GHD), lambda g, p: (g, part(p), 0, 0))
    gz_s = pl.BlockSpec((rows, wide), lambda g, p: (part(p), BLK_GZ // nh + g))
    mix_s = pl.BlockSpec((rows, wide), lambda g, p: (part(p), NPAIR // nh + g))
    return hs, a_s, gl_s, st_s, gz_s, mix_s


def _head_cols(hh):
    return slice(hh * GHD, (hh + 1) * GHD)


SCAN_HEADS, SCAN_PARTS = 4, 2
SCAN_HEADS_BWD, SCAN_PARTS_BWD = 4, 4


def _gdn_scan(u, w, qd, kd, a, gl, proj, w_norm, mix):
    heads = range(SCAN_HEADS)

    def body(u_ref, w_ref, qd_ref, kd_ref, a_ref, gl_ref, z_ref, wn_ref, mix_in, mix_ref, o_ref, st_ref, carry_ref):
        del mix_in

        @pl.when(pl.program_id(1) == 0)
        def _():
            carry_ref[...] = jnp.zeros_like(carry_ref)

        def step(ci, states):
            rows = pl.ds(pl.multiple_of(ci * CHUNK, CHUNK), CHUNK)
            for hh in heads:
                st_ref[hh, ci] = states[hh]
            ws = [_dot(w_ref[rows, _head_cols(hh)], states[hh]) for hh in heads]
            qs = [_dot(qd_ref[rows, _head_cols(hh)], states[hh]) for hh in heads]
            vn = [u_ref[rows, _head_cols(hh)] - ws[hh] for hh in heads]
            av = [_dot(a_ref[hh, rows, :], vn[hh]) for hh in heads]
            kv = [_dot(kd_ref[rows, _head_cols(hh)], vn[hh], 0, 0) for hh in heads]
            for hh in heads:
                o_ref[rows, _head_cols(hh)] = qs[hh] + av[hh]
            return tuple(states[hh] * gl_ref[hh, ci] + kv[hh] for hh in heads)

        last = lax.fori_loop(0, NCH // SCAN_PARTS, step, tuple(carry_ref[hh] for hh in heads))
        for hh in heads:
            carry_ref[hh] = last[hh]
            ov = o_ref[:, _head_cols(hh)]
            mix_ref[:, _head_cols(hh)] = (ov * _rms_scale(ov) * wn_ref[...] * _silu(z_ref[:, _head_cols(hh)])).astype(BF16)

    hs, a_s, gl_s, st_s, gz_s, mix_s = _scan_specs(SCAN_HEADS, SCAN_PARTS, False)
    return pl.pallas_call(
        body, name="gdn_scan", grid=(NGH // SCAN_HEADS, SCAN_PARTS),
        in_specs=[hs, hs, hs, hs, a_s, gl_s, gz_s, pl.BlockSpec((1, GHD), lambda g, p: (0, 0)),
                  pl.BlockSpec(memory_space=pl.ANY)],
        out_specs=[mix_s, hs, st_s],
        out_shape=[jax.ShapeDtypeStruct((S, D), BF16), jax.ShapeDtypeStruct((S, DGDN), F32),
                   jax.ShapeDtypeStruct((NGH, NCH, GHD, GHD), F32)],
        input_output_aliases={8: 0}, scratch_shapes=[pltpu.VMEM((SCAN_HEADS, GHD, GHD), F32)], compiler_params=_cparams(),
    )(u, w, qd, kd, a, gl, proj, w_norm, mix)


def _gdn_scan_bwd(dmix, o, proj, w_norm, u, w, qd, kd, a, gl, states, dproj, exchange=None):
    ex_in, ex_in_specs, ex_out_specs, ex_out_shape, ex_scratch = _hosted(exchange)
    groups = NGH // SCAN_HEADS_BWD

    def body(*refs):
        dy_ref, o_ref, z_ref, wn_ref, u_ref, w_ref, qd_ref, kd_ref, a_ref, gl_ref, st_ref = refs[:11]
        dz_ref, du_ref, dw_ref, dqd_ref, dkd_ref, da_ref, dgl_ref, dwn_ref = refs[12 + len(ex_in):20 + len(ex_in)]
        do_ref, carry_ref = refs[20 + len(ex_in) + len(ex_out_shape):22 + len(ex_in) + len(ex_out_shape)]
        ex_refs = refs[12:12 + len(ex_in)] + refs[20 + len(ex_in):20 + len(ex_in) + len(ex_out_shape)] + refs[-2:]
        heads = range(SCAN_HEADS_BWD)
        chunks = NCH // SCAN_PARTS_BWD

        if exchange is not None:
            @pl.when((pl.program_id(0) == 0) & (pl.program_id(1) == 0))
            def _():
                exchange.start(*exchange.split(ex_refs))

        @pl.when((pl.program_id(0) == 0) & (pl.program_id(1) == 0))
        def _():
            dwn_ref[...] = jnp.zeros_like(dwn_ref)

        @pl.when(pl.program_id(1) == 0)
        def _():
            carry_ref[...] = jnp.zeros_like(carry_ref)

        wn = wn_ref[...]
        for hh in heads:
            c = _head_cols(hh)
            ov = o_ref[:, c]
            zv = z_ref[:, c]
            g = dy_ref[:, c]
            sig = _sigmoid(zv)
            dz_ref[:, c] = (g * (ov * _rms_scale(ov) * wn) * sig * (1.0 + zv * (1.0 - sig))).astype(BF16)
            do, dwt = _rms_bwd(ov, wn, g * zv * sig)
            do_ref[:, c] = do
            dwn_ref[...] += jnp.sum(dwt, axis=0, keepdims=True)

        def step(t, dstates):
            ci = chunks - 1 - t
            rows = pl.ds(pl.multiple_of(ci * CHUNK, CHUNK), CHUNK)
            cols = [_head_cols(hh) for hh in heads]
            state = [st_ref[hh, ci] for hh in heads]
            dov = [do_ref[rows, cols[hh]] for hh in heads]
            wv = [w_ref[rows, cols[hh]] for hh in heads]
            ws = [_dot(wv[hh], state[hh]) for hh in heads]
            adov = [_dot(a_ref[hh, rows, :], dov[hh], 0, 0) for hh in heads]
            kds = [_dot(kd_ref[rows, cols[hh]], dstates[hh]) for hh in heads]
            dqd = [_dot(dov[hh], state[hh], 1, 1) for hh in heads]
            qdo = [_dot(qd_ref[rows, cols[hh]], dov[hh], 0, 0) for hh in heads]
            vn = [u_ref[rows, cols[hh]] - ws[hh] for hh in heads]
            dvn = [adov[hh] + kds[hh] for hh in heads]
            da = [_dot(dov[hh], vn[hh], 1, 1) for hh in heads]
            dkd = [_dot(vn[hh], dstates[hh], 1, 1) for hh in heads]
            dwv = [_dot(dvn[hh], state[hh], 1, 1) for hh in heads]
            wdv = [_dot(wv[hh], dvn[hh], 0, 0) for hh in heads]
            for hh in heads:
                da_ref[hh, rows, :] = da[hh]
                dqd_ref[rows, cols[hh]] = dqd[hh]
                dkd_ref[rows, cols[hh]] = dkd[hh]
                dgl = jnp.sum(jnp.sum(dstates[hh] * state[hh], axis=1, keepdims=True), axis=0, keepdims=True)
                dgl_ref[hh, ci] = jnp.broadcast_to(dgl, (1, LANES))
                du_ref[rows, cols[hh]] = dvn[hh]
                dw_ref[rows, cols[hh]] = -dwv[hh]
            return tuple(dstates[hh] * gl_ref[hh, ci] + qdo[hh] - wdv[hh] for hh in heads)

        last = lax.fori_loop(0, chunks, step, tuple(carry_ref[hh] for hh in heads))
        for hh in heads:
            carry_ref[hh] = last[hh]

        if exchange is not None:
            @pl.when((pl.program_id(0) == groups - 1) & (pl.program_id(1) == SCAN_PARTS_BWD - 1))
            def _():
                exchange.finish(*exchange.split(ex_refs))

    hs, a_s, gl_s, st_s, gz_s, mix_s = _scan_specs(SCAN_HEADS_BWD, SCAN_PARTS_BWD, True)
    vec = pl.BlockSpec((1, GHD), lambda g, p: (0, 0))
    tok = jax.ShapeDtypeStruct((S, DGDN), F32)
    res = pl.pallas_call(
        body, name="gdn_scan_bwd", grid=(groups, SCAN_PARTS_BWD),
        in_specs=[mix_s, hs, gz_s, vec, hs, hs, hs, hs, a_s, gl_s, st_s, pl.BlockSpec(memory_space=pl.ANY)] + ex_in_specs,
        out_specs=[gz_s, hs, hs, hs, hs, a_s, gl_s, vec] + ex_out_specs,
        out_shape=[jax.ShapeDtypeStruct((S, DPROJ_PAD), BF16), tok, tok, tok, tok,
                   jax.ShapeDtypeStruct((NGH, S, CHUNK), F32), jax.ShapeDtypeStruct((NGH, NCH, 1, LANES), F32),
                   jax.ShapeDtypeStruct((1, GHD), F32)] + ex_out_shape,
        input_output_aliases={11: 0},
        scratch_shapes=[pltpu.VMEM((S // SCAN_PARTS_BWD, SCAN_HEADS_BWD * GHD), F32),
                        pltpu.VMEM((SCAN_HEADS_BWD, GHD, GHD), F32)] + ex_scratch,
        compiler_params=_cparams(),
    )(dmix, o, proj, w_norm, u, w, qd, kd, a, gl, states, dproj, *ex_in)
    return res[:8], res[8:]


def _place():
    return lax.axis_index("x"), lax.axis_index("y"), lax.axis_index("c")


def _place_scalars():
    x, y, c = _place()
    return jnp.stack([2 * x + y, c]).astype(jnp.int32)


def _other_chips(x, y):
    return [(1 - x, y), (x, 1 - y), (1 - x, 1 - y)]


HBM = pl.BlockSpec(memory_space=pltpu.HBM)
VMEM = pl.BlockSpec(memory_space=pltpu.VMEM)


def _half_rows(ref_or_rows, half):
    rows = ref_or_rows // 2
    return pl.ds(pl.multiple_of(half * rows, rows), rows)


class _Exchange:
    def __init__(self, inputs, out_shape, n_sems, start, finish=None, middle=None, rest=None):
        self.inputs, self.out_shape, self.n_sems, self.start = inputs, out_shape, n_sems, start
        if finish is None:
            def finish(*refs):
                middle(*refs)
                rest(*refs)
        self.finish = finish
        self.middle = middle if middle is not None else (lambda *refs: None)
        self.rest = rest if rest is not None else finish

    def sem_shapes(self):
        return [pltpu.SemaphoreType.DMA((self.n_sems,)), pltpu.SemaphoreType.DMA((self.n_sems,))]

    def split(self, refs):
        n_in, n_out = len(self.inputs), len(self.out_shape)
        return refs[:n_in], refs[n_in:n_in + n_out], refs[n_in + n_out], refs[n_in + n_out + 1]


def _allgather_exchange(shards, whole=(), sent_rows=None):
    n, nw = len(shards), len(whole)
    slots = 8
    sent_rows = [s.shape[0] for s in shards] if sent_rows is None else sent_rows

    def plan(c, src, outs, send_sems, recv_sems):
        x, y, _ = _place()
        via_x, via_y, diagonal = _other_chips(x, y)
        id_x, id_y, id_diagonal = [2 * chip[0] + chip[1] for chip in (via_x, via_y, diagonal)]
        me, sibling = (x, y, c), (x, y, 1 - c)

        def rows_of(a, half, quarter):
            total = src[a].shape[0]
            first = half * (total // 2) + (0 if quarter is None else quarter * (total // 4))
            size = min(total // 2 if quarter is None else total // 4, sent_rows[a] - first)
            assert size > 0 and size % BF16_ROWS == 0, (a, half, quarter, size)
            return pl.ds(first, size)

        def copy(a, k, chip_index, half, quarter, to, from_src=False):
            rows = rows_of(a, half, quarter)
            dst = outs[a].at[chip_index, rows]
            return pltpu.make_async_remote_copy(
                src_ref=src[a].at[rows] if from_src else dst, dst_ref=dst, send_sem=send_sems.at[slots * a + k],
                recv_sem=recv_sems.at[slots * a + k], device_id=to, device_id_type=MESH)

        def whole_copy(b, k, chip_index, to):
            return pltpu.make_async_remote_copy(
                src_ref=src[n + b], dst_ref=outs[n + b].at[chip_index], send_sem=send_sems.at[slots * n + 3 * b + k],
                recv_sem=recv_sems.at[slots * n + 3 * b + k], device_id=to, device_id_type=MESH)

        first, stages, last = [], [], []
        for a in range(n):
            first += [copy(a, 0, 2 * x + y, c, None, (*via_x, c), True), copy(a, 1, 2 * x + y, c, None, (*via_y, c), True)]
            stages.append([
                (copy(a, 0, id_x, c, None, me),
                 [copy(a, 2, id_x, c, 0, (*via_y, c)), copy(a, 4, id_x, c, None, sibling)]),
                (copy(a, 1, id_y, c, None, me),
                 [copy(a, 3, id_y, c, 1, (*via_x, c)), copy(a, 5, id_y, c, None, sibling)]),
                (copy(a, 2, id_diagonal, c, 0, me), [copy(a, 6, id_diagonal, c, 0, sibling)]),
                (copy(a, 3, id_diagonal, c, 1, me), [copy(a, 7, id_diagonal, c, 1, sibling)]),
            ])
            last += [copy(a, 4, id_x, 1 - c, None, me), copy(a, 5, id_y, 1 - c, None, me),
                     copy(a, 6, id_diagonal, 1 - c, 0, me), copy(a, 7, id_diagonal, 1 - c, 1, me)]
        for b in range(nw):
            for k, (chip, index) in enumerate(((via_x, id_x), (via_y, id_y), (diagonal, id_diagonal))):
                first.append(whole_copy(b, k, 2 * x + y, (*chip, c)))
                last.append(whole_copy(b, k, index, me))
        return first, stages, last

    def each_core(step):
        def run(*refs):
            for core in (0, 1):
                @pl.when(lax.axis_index("c") == core)
                def _(core=core):
                    step(core, *refs)

        return run

    def start(core, *refs):
        for cp in plan(core, *refs)[0]:
            cp.start()

    def pass_on(stages, which):
        for stage in which:
            for per_shard in stages:
                lands, onward = per_shard[stage]
                lands.wait_recv()
                for cp in onward:
                    cp.start()

    def middle(core, *refs):
        pass_on(plan(core, *refs)[1], (0, 1))

    def rest(core, *refs):
        first, stages, last = plan(core, *refs)
        pass_on(stages, (2, 3))
        for cp in last:
            cp.wait_recv()
        for cp in first + [cp for per_shard in stages for _, onward in per_shard for cp in onward]:
            cp.wait_send()

    out_shape = [jax.ShapeDtypeStruct((NCHIP,) + s.shape, s.dtype) for s in list(shards) + list(whole)]
    return _Exchange(list(shards) + list(whole), out_shape, slots * n + 3 * nw, each_core(start),
                     middle=each_core(middle), rest=each_core(rest))


def _with_own(gathered, own):
    x, y, _ = _place()
    return lax.dynamic_update_index_in_dim(gathered, own, 2 * x + y, axis=0)


def _simple_exchange(inputs, out_shape, copies_of):
    def start(*refs):
        for cp in copies_of(*refs):
            cp.start()

    def finish(*refs):
        for cp in copies_of(*refs):
            cp.wait()

    return _Exchange(list(inputs), out_shape, len(out_shape) * 3, start, finish)


def _pair_exchange(grads):
    def copies_of(src, outs, send_sems, recv_sems):
        x, y, c = _place()
        return [pltpu.make_async_remote_copy(
            src_ref=src[a].at[:, _half_rows(src[a].shape[1], 1 - c)], dst_ref=outs[a], send_sem=send_sems.at[a],
            recv_sem=recv_sems.at[a], device_id=(x, y, 1 - c), device_id_type=MESH) for a in range(len(src))]

    return _simple_exchange(
        grads, [jax.ShapeDtypeStruct((g.shape[0], g.shape[1] // 2, g.shape[2]), g.dtype) for g in grads], copies_of)


def _pair_sum(grads, theirs, name):
    n = len(grads)

    def body(place_ref, *refs):
        for a in range(n):
            refs[2 * n + a][...] = (refs[a][...].astype(F32) + refs[n + a][...].astype(F32)).astype(BF16)

    def specs(arrs):
        return [pl.BlockSpec((None,) + g.shape[1:], lambda j, place: (j, 0, 0)) for g in arrs]

    own_half = [pl.BlockSpec((None, g.shape[1] // 2, g.shape[2]), lambda j, place: (j, place[1], 0)) for g in grads]
    return pl.pallas_call(
        body, name=name, grid_spec=pltpu.PrefetchScalarGridSpec(
            num_scalar_prefetch=1, grid=(NCHIP,), in_specs=own_half + specs(theirs), out_specs=specs(theirs)),
        out_shape=[jax.ShapeDtypeStruct(g.shape, BF16) for g in theirs], compiler_params=_cparams(),
    )(_place_scalars(), *grads, *theirs)


def _chip_exchange(parts, rows_by_core=None):
    if rows_by_core is None:
        rows_by_core = [(p.shape[1],) * 2 for p in parts]

    def each_core(act):
        def run(src, outs, send_sems, recv_sems):
            x, y, c = _place()
            for core in (0, 1):
                @pl.when(c == core)
                def _(core=core):
                    for a in range(len(src)):
                        rows = pl.ds(0, rows_by_core[a][core])
                        for k, chip in enumerate(_other_chips(x, y)):
                            act(pltpu.make_async_remote_copy(
                                src_ref=src[a].at[2 * chip[0] + chip[1], rows], dst_ref=outs[a].at[k, rows],
                                send_sem=send_sems.at[3 * a + k], recv_sem=recv_sems.at[3 * a + k],
                                device_id=(*chip, core), device_id_type=MESH))

        return run

    return _Exchange(list(parts), [jax.ShapeDtypeStruct((NCHIP - 1,) + p.shape[1:], p.dtype) for p in parts],
                     3 * len(parts), each_core(lambda cp: cp.start()), each_core(lambda cp: cp.wait()))


def _chip_sum(parts, received, exchange=None):
    n = len(parts)
    steps = 4
    ex_in, ex_in_specs, ex_out_specs, ex_out_shape, ex_scratch = _hosted(exchange)
    n_ex = len(ex_in)

    def body(place_ref, *refs):
        mine, theirs = refs[2 * n + n_ex:3 * n + n_ex], refs[3 * n + n_ex:4 * n + n_ex]
        ex_refs = refs[2 * n:2 * n + n_ex] + refs[4 * n + n_ex:len(refs) - n - 2]
        tiles, send_sems, recv_sems = refs[len(refs) - n - 2:len(refs) - 2], refs[-2], refs[-1]
        step = pl.program_id(0)
        if exchange is not None:
            @pl.when(step == 0)
            def _():
                exchange.start(*exchange.split(ex_refs))

        x, y, c = _place()

        def share(a, i):
            rows = tiles[a].shape[1]
            return pltpu.make_async_remote_copy(
                src_ref=tiles[a].at[i], dst_ref=theirs[a].at[pl.ds(pl.multiple_of(i * rows, rows), rows)],
                send_sem=send_sems.at[a * steps + i], recv_sem=recv_sems.at[a * steps + i], device_id=(x, y, 1 - c),
                device_id_type=MESH)

        for a in range(n):
            own, r = refs[a], refs[n + a]
            total = ((own[...].astype(F32) + r[0].astype(F32)) + r[1].astype(F32)) + r[2].astype(F32)
            mine[a][...] = total
            tiles[a][step] = total
            share(a, step).start()

        @pl.when(step == steps - 1)
        def _():
            if exchange is not None:
                exchange.finish(*exchange.split(ex_refs))
            for a in range(n):
                for i in range(steps):
                    share(a, i).wait()

    own_specs = [pl.BlockSpec((None, g.shape[1] // steps, g.shape[2]), lambda i, place: (place[0], i, 0)) for g in parts]
    received_specs = [pl.BlockSpec((g.shape[0], g.shape[1] // steps, g.shape[2]), lambda i, place: (0, i, 0))
                      for g in received]
    out_specs = [pl.BlockSpec((g.shape[1] // steps, g.shape[2]), lambda i, place: (i, 0)) for g in parts]
    halves = [jax.ShapeDtypeStruct(g.shape[1:], F32) for g in parts]
    res = pl.pallas_call(
        body, name="grads_chip_sum", grid_spec=pltpu.PrefetchScalarGridSpec(
            num_scalar_prefetch=1, grid=(steps,), in_specs=own_specs + received_specs + ex_in_specs,
            out_specs=out_specs + [HBM] * n + ex_out_specs,
            scratch_shapes=ex_scratch + [pltpu.VMEM((steps, g.shape[1] // steps, g.shape[2]), F32) for g in parts]
            + [pltpu.SemaphoreType.DMA((n * steps,))] * 2),
        out_shape=halves * 2 + ex_out_shape, compiler_params=_cparams(),
    )(_place_scalars(), *parts, *received, *ex_in)
    return res[:n], res[n:2 * n], res[2 * n:]


def _adamw_math(w, g, m, v):
    nm = ADAM_B1 * m + (1.0 - ADAM_B1) * g
    nv = ADAM_B2 * v + (1.0 - ADAM_B2) * jnp.square(g)
    m_hat = nm / (1.0 - ADAM_B1 ** ADAM_STEP)
    v_hat = nv / (1.0 - ADAM_B2 ** ADAM_STEP)
    return -ADAM_LR * (m_hat / (jnp.sqrt(v_hat) + ADAM_EPS) + ADAM_WD * w), nm, nv


def _adamw_big(ws, g_mine, g_theirs, ms, vs):
    n = len(ws)
    steps = 8
    per_half = steps // 2

    def body(place_ref, *refs):
        outs = refs[5 * n:]
        own_half = (pl.program_id(0) // per_half) == place_ref[1]
        for a in range(n):
            g = jnp.where(own_half, refs[n + a][...], refs[2 * n + a][...])
            d, nm, nv = _adamw_math(refs[a][...], g, refs[3 * n + a][...], refs[4 * n + a][...])
            outs[a][...] = g
            outs[n + a][...] = d
            outs[2 * n + a][...] = nm
            outs[3 * n + a][...] = nv

    specs = [pl.BlockSpec((w.shape[0] // steps, w.shape[1]), lambda i, place: (i, 0)) for w in ws]

    def half_specs(halves, of_this_core):
        def tile(i, place):
            first = (place[1] if of_this_core else 1 - place[1]) * per_half
            return jnp.clip(i - first, 0, per_half - 1), 0

        return [pl.BlockSpec((g.shape[0] // per_half, g.shape[1]), tile) for g in halves]

    shapes = [jax.ShapeDtypeStruct(w.shape, F32) for w in ws]
    res = pl.pallas_call(
        body, name="adamw_big", grid_spec=pltpu.PrefetchScalarGridSpec(
            num_scalar_prefetch=1, grid=(steps,),
            in_specs=specs + half_specs(g_mine, True) + half_specs(g_theirs, False) + specs * 2, out_specs=specs * 4),
        out_shape=shapes * 4, compiler_params=_cparams(),
    )(_place_scalars(), *ws, *g_mine, *g_theirs, *ms, *vs)
    return res[:n], res[n:2 * n], res[2 * n:3 * n], res[3 * n:]


def _adamw_in(w, g_mine, g_theirs, m, v):
    half = D // 2

    def body(w_ref, gm_ref, gt_ref, m_ref, v_ref, g_out, d_out, nm_out, nv_out, g_ref):
        south = lax.axis_index("c") == 0
        g_ref[0:half, :] = jnp.where(south, gm_ref[...], gt_ref[...])
        g_ref[half:D, :] = jnp.where(south, gt_ref[...], gm_ref[...])
        g = g_ref[0:CW, :]
        d, nm, nv = _adamw_math(w_ref[...], g, m_ref[...], v_ref[...])
        g_out[...] = g
        d_out[...] = d
        nm_out[...] = nm
        nv_out[...] = nv

    cols = 2 * LANES
    spec = pl.BlockSpec((CW, cols), lambda i: (0, i))
    half_spec = pl.BlockSpec((half, cols), lambda i: (0, i))
    return pl.pallas_call(
        body, name="adamw_in", grid=(D // cols,), in_specs=[spec, half_spec, half_spec, spec, spec], out_specs=[spec] * 4,
        out_shape=[jax.ShapeDtypeStruct((CW, D), F32)] * 4, scratch_shapes=[pltpu.VMEM((D, cols), F32)],
        compiler_params=_cparams(),
    )(w, g_mine, g_theirs, m, v)


NORM_NAMES = ("pre_mix_norm", "post_mix_norm", "pre_mlp_norm", "post_mlp_norm")
SMALL_NAMES = NORM_NAMES + ("gdn_conv_w", "fox_f_bias", "gdn_dt_bias", "gdn_a_log", "fox_out_norm", "gdn_out_norm")
CONV_COLS = 3 * DGDN // NCHIP


def _small_gather(d_norms, d_conv, sums, d_fox_norm, d_gdn_norm, loss_row):
    n_arrays = 6
    n_remote = n_arrays * (NDEV - 1)

    def copies_of(src, outs, send_sems, recv_sems):
        x, y, c = _place()
        me = 4 * x + 2 * y + c

        def from_me(chip_index):
            cols = pl.ds(pl.multiple_of(chip_index * CONV_COLS, LANES), CONV_COLS)
            return [src[0], src[1].at[:, cols], src[2], src[3], src[4], src[5]]

        local = [pltpu.make_async_copy(s, outs[a].at[me], send_sems.at[n_remote + a]) for a, s in enumerate(from_me(2 * x + y))]
        remote = []
        for k in range(1, NDEV):
            px, py, pc = x ^ ((k >> 2) & 1), y ^ ((k >> 1) & 1), c ^ (k & 1)
            remote += [pltpu.make_async_remote_copy(
                src_ref=s, dst_ref=outs[a].at[me], send_sem=send_sems.at[n_arrays * (k - 1) + a],
                recv_sem=recv_sems.at[n_arrays * (k - 1) + a], device_id=(px, py, pc), device_id_type=MESH)
                for a, s in enumerate(from_me(2 * px + py))]
        return local + remote

    def start(*refs):
        for cp in copies_of(*refs):
            cp.start()

    def finish(*refs):
        for cp in copies_of(*refs):
            cp.wait()

    shapes = [(4, D), (CONV_K, CONV_COLS), (8, LANES), (1, LANES), (1, LANES), (1, LANES)]
    return _Exchange([d_norms, d_conv, sums, d_fox_norm, d_gdn_norm, loss_row],
                     [jax.ShapeDtypeStruct((NDEV,) + s, F32) for s in shapes], n_remote + n_arrays, start, finish)


def _small_adamw(gathered, ws, ms, vs):
    n = len(SMALL_NAMES)
    ng = len(gathered)

    def body(*refs):
        def total(buf):
            acc = buf[0]
            for i in range(1, NDEV):
                acc = acc + buf[i]
            return acc

        t_norms, t_conv, t_sums, t_fn, t_gn, t_loss = [total(r) for r in refs[:ng]]
        w_refs, m_refs, v_refs = refs[ng:ng + n], refs[ng + n:ng + 2 * n], refs[ng + 2 * n:ng + 3 * n]
        outs = refs[ng + 3 * n:]
        outs[4 * n][...] = t_loss
        grads = [t_norms[i:i + 1, :] for i in range(4)] + [
            t_conv, t_sums[0:1, 0:NFH], t_sums[1:2, 0:NGH], t_sums[2:3, 0:NGH], t_fn[:, 0:FHD], t_gn]
        for a in range(n):
            d, nm, nv = _adamw_math(w_refs[a][...], grads[a], m_refs[a][...], v_refs[a][...])
            outs[a][...] = grads[a]
            outs[n + a][...] = d
            outs[2 * n + a][...] = nm
            outs[3 * n + a][...] = nv

    def whole(arr):
        return pl.BlockSpec(arr.shape, lambda i: (0,) * arr.ndim)

    res = pl.pallas_call(
        body, name="small_adamw", grid=(1,), in_specs=[whole(t) for t in gathered] + [whole(w) for w in ws] * 3,
        out_specs=[whole(w) for w in ws] * 4 + [pl.BlockSpec((1, LANES), lambda i: (0, 0))],
        out_shape=[jax.ShapeDtypeStruct(w.shape, F32) for w in ws] * 4 + [jax.ShapeDtypeStruct((1, LANES), F32)],
        compiler_params=_cparams(),
    )(*gathered, *ws, *ms, *vs)
    return res[:n], res[n:2 * n], res[2 * n:3 * n], res[3 * n:4 * n], res[4 * n]


CW = DPROJ // NCHIP
BF16_ROWS = 16
CW_SENT = -(-CW // BF16_ROWS) * BF16_ROWS
PROJ_RUNS = tuple((part * DFOX + hp * LANES, part * DFOX + (hp + 1) * LANES, (3 * hp + part) * LANES)
                  for hp in range(NPAIR) for part in range(3)) + (
    (1536, 1544, BLK_SMALL * LANES), (1544, 3080, BLK_GDN * LANES), (3080, 3088, BLK_SMALL * LANES + 8),
    (3088, 3600, BLK_GZ * LANES))


def _proj_pieces():
    pieces = []
    for lo, hi, at in PROJ_RUNS:
        while lo < hi:
            j = lo // CW
            end = min(hi, (j + 1) * CW)
            pieces.append((j, lo - j * CW, at, end - lo))
            at, lo = at + end - lo, end
    return pieces


RT = 256


def _to_padded_rows(gathered):
    def body(src_ref, out_ref, blocks_ref, rows_ref):
        blocks_ref[...] = src_ref[...].astype(F32)
        rows_ref[...] = jnp.zeros_like(rows_ref)
        for j, start, at, n in _proj_pieces():
            rows_ref[at:at + n, :] = blocks_ref[j, start:start + n, :]
        out_ref[...] = rows_ref[...].astype(out_ref.dtype)

    return pl.pallas_call(
        body, name="proj_rows_in", grid=(D // RT,), in_specs=[pl.BlockSpec((NCHIP, D, RT), lambda i: (0, 0, i))],
        out_specs=pl.BlockSpec((DPROJ_PAD, RT), lambda i: (0, i)), out_shape=jax.ShapeDtypeStruct((DPROJ_PAD, D), gathered.dtype),
        scratch_shapes=[pltpu.VMEM((NCHIP, D, RT), F32), pltpu.VMEM((DPROJ_PAD, RT), F32)], compiler_params=_cparams(),
    )(gathered)


def _from_padded_rows_pair_sum(w):
    steps = D // RT
    half = D // 2

    def body(src_ref, out_ref, rows_ref, blocks_ref, mine_ref, send_ref, recv_ref, send_sems, recv_sems):
        i = pl.program_id(0)
        x, y, c = _place()

        def share(t):
            return pltpu.make_async_remote_copy(
                src_ref=send_ref.at[t], dst_ref=recv_ref.at[t], send_sem=send_sems.at[t], recv_sem=recv_sems.at[t],
                device_id=(x, y, 1 - c), device_id_type=MESH)

        def rows_of(core):
            return blocks_ref[:, pl.ds(pl.multiple_of(core * half, half), half), :]

        @pl.when(i < steps)
        def _():
            rows_ref[...] = src_ref[...].astype(F32)
            blocks_ref[...] = jnp.zeros_like(blocks_ref)
            for j, start, at, n in _proj_pieces():
                blocks_ref[j, start:start + n, :] = rows_ref[at:at + n, :]
            mine_ref[i % 2] = rows_of(c)
            send_ref[i] = rows_of(1 - c).astype(BF16)
            share(i).start()

        @pl.when(i > 0)
        def _():
            share(i - 1).wait_recv()
            out_ref[...] = (mine_ref[(i - 1) % 2] + recv_ref[i - 1].astype(F32)).astype(BF16)

        @pl.when(i == steps)
        def _():
            for t in range(steps):
                share(t).wait_send()

    tile = (NCHIP, half, RT)
    return pl.pallas_call(
        body, name="proj_rows_out_pair_sum", grid=(steps + 1,),
        in_specs=[pl.BlockSpec((DPROJ_PAD, RT), lambda i: (0, jnp.minimum(i, steps - 1)))],
        out_specs=pl.BlockSpec(tile, lambda i: (0, 0, jnp.maximum(i - 1, 0))),
        out_shape=jax.ShapeDtypeStruct((NCHIP, half, D), BF16),
        scratch_shapes=[pltpu.VMEM((DPROJ_PAD, RT), F32), pltpu.VMEM((NCHIP, D, RT), F32), pltpu.VMEM((2,) + tile, F32),
                        pltpu.VMEM((steps,) + tile, BF16), pltpu.VMEM((steps,) + tile, BF16),
                        pltpu.SemaphoreType.DMA((steps,)), pltpu.SemaphoreType.DMA((steps,))],
        compiler_params=_cparams(),
    )(w)


def _local_step(x, target, first_weights, late_weights, reduce_late, reduce_in, pre_mix_norm, fox_f_bias, fox_out_norm,
                gdn_a_log, gdn_dt_bias, gdn_out_norm, post_mix_norm, pre_mlp_norm, post_mlp_norm):
    bias_vec = jnp.zeros((1, LANES), F32).at[0, 0:NFH].set(fox_f_bias).at[0, LANE_G:LANE_G + NGH].set(gdn_dt_bias)
    alog_vec = jnp.zeros((1, LANES), F32).at[0, LANE_G:LANE_G + NGH].set(gdn_a_log)
    w2 = jnp.concatenate([fox_out_norm, fox_out_norm], axis=1)

    h, first = _pre_norm(x, pre_mix_norm, exchange=first_weights[0])
    win_p, conv_w = first_weights[1](first)
    proj = _matmul(h, win_p, tb=True, tm=2048, tn=768, tk=1024, name="mm_proj", exchange=late_weights[0])
    proj, late_a = proj if late_weights[0] is not None else (proj, [])
    gates = _gates(proj, bias_vec, alog_vec)
    mix, fox_o, lse, late_b = _fox_fwd(proj, gates, w2, exchange=late_weights[1])
    qkv = _gdn_pre(proj, conv_w)
    (u, w, qd, kd, a_intra, gl, t_inv), _ = _gdn_prep(qkv, gates)
    wout, wup3 = late_weights[3](late_a, late_b)
    mix, gdn_raw, states = _gdn_scan(u, w, qd, kd, a_intra, gl, proj, gdn_out_norm, mix)
    def post_mix(acc, xv, w_post, w_pre_mlp):
        x1v = xv + acc * _rms_scale(acc) * w_post
        return acc, x1v, x1v * _rms_scale(x1v) * w_pre_mlp

    mixed, x1, h2 = _matmul(mix, wout, tm=512, tn=D, tk=1024, out_dtypes=(F32, F32, BF16), name="mm_out",
                            extra=(x, post_mix_norm, pre_mlp_norm), epilogue=post_mix)

    def relu2(acc):
        r = jnp.maximum(acc, 0.0)
        return r, r * r

    up_act = _matmul(h2, wup3, b3=True, tm=1024, tn=1024, tk=1024, out_dtypes=(BF16, BF16), epilogue=relu2,
                     name="mm_up", exchange=late_weights[2])
    (up_relu, act), late_c = up_act if late_weights[2] is not None else (up_act, [])
    wdown = late_weights[4](late_c)
    def loss_head(acc, x1v, tv, w):
        err = x1v + acc * _rms_scale(acc) * w - tv
        dx2v = err * (1.0 / D)
        dyv, dwt = _rms_bwd(acc, w, dx2v)
        part = 0.5 * jnp.sum(jnp.mean(err * err, axis=-1, keepdims=True), axis=0, keepdims=True)
        return dx2v, dyv, jnp.sum(dwt, axis=0, keepdims=True), jnp.broadcast_to(part, (1, D))

    dx2, dy, d_post_mlp, loss_wide = _matmul(
        act, wdown, tm=512, tn=D, tk=DFF, out_dtypes=(F32, BF16, F32, F32), extra=(x1, target, post_mlp_norm),
        epilogue=loss_head, n_sums=2, lag=True, name="mm_down")
    loss_row = loss_wide[:, :LANES]

    dwdown = _matmul(act, dy, ta=True, tm=1024, tn=1024, tk=2048, out_dtypes=(BF16,), name="mm_dwdown")

    def relu2_bwd(acc, r):
        return (acc * 2.0 * r.astype(F32),)

    dup = _matmul(dy, wdown, tb=True, tm=1024, tn=1024, tk=1024, out_dtypes=(BF16,), extra=(up_relu,), epilogue=relu2_bwd,
                  name="mm_dact")
    dwup3 = _matmul(h2, dup, ta=True, tm=1024, tn=1024, tk=2048, out_dtypes=(BF16,), o3=True, name="mm_dwup")
    def mid_bwd(acc, x1v, dx2v, mixedv, w_pre_mlp, w_post):
        dxa, dwm = _rms_bwd(x1v, w_pre_mlp, acc)
        dx1v = dx2v + dxa
        dm, dwp = _rms_bwd(mixedv, w_post, dx1v)
        return dx1v, dm, jnp.sum(dwm, axis=0, keepdims=True), jnp.sum(dwp, axis=0, keepdims=True)

    dx1, dmixed, d_pre_mlp, d_post_mix = _matmul(
        dup, wup3, tb=True, b3=True, tm=512, tn=D, tk=DFF, out_dtypes=(F32, BF16, F32, F32),
        extra=(x1, dx2, mixed, pre_mlp_norm, post_mix_norm), epilogue=mid_bwd, n_sums=2, name="mm_dh2")
    dwout = _matmul(mix, dmixed, ta=True, tm=256, tn=1024, tk=2048, out_dtypes=(BF16,), name="mm_dwout")
    dmix = _matmul(dmixed, wout, tb=True, tm=512, tn=1024, tk=1024, name="mm_dmix")

    dfox, delta, d_fox_norm, from_sibling = _fox_norm_bwd(fox_o, dmix, w2, exchange=reduce_late[0](dwout, dwup3, dwdown))
    dproj, dcum_fox, reduced_a = _fox_bwd(proj, dfox, gates, lse, delta, exchange=reduce_late[1](from_sibling))
    (dproj, du, dw, dqd, dkd, da, dgl, d_gdn_norm), reduced_b = _gdn_scan_bwd(
        dmix, gdn_raw, proj, gdn_out_norm, u, w, qd, kd, a_intra, gl, states, dproj, exchange=reduce_late[2]())
    dqkv, dgates_gdn, reduced_c = _gdn_prep_bwd(qkv, gates, t_inv, du, dw, dqd, dkd, da, dgl, exchange=reduce_late[3]())
    reduced_late = (reduced_a, reduced_b, reduced_c)
    dproj, d_conv = _gdn_pre_bwd(proj, conv_w, dqkv, dproj)
    dproj, sums = _gates_bwd(proj, bias_vec, alog_vec, dgates_gdn, dcum_fox, dproj)

    dwin_p = _matmul(dproj, h, ta=True, tm=1280, tn=1024, tk=2048, out_dtypes=(BF16,), name="mm_dwin")
    exchange_in = reduce_in(dwin_p)
    dh = _matmul(dproj, win_p, tm=1024, tk=DPROJ_PAD, name="mm_dh", exchange=exchange_in)
    dh, reduced_in = dh if exchange_in is not None else (dh, [])
    grad_x, d_pre_mix = _pre_norm_bwd(dh, x, pre_mix_norm, dx1)

    d_norms = jnp.concatenate([d_pre_mix, d_post_mix, d_pre_mlp, d_post_mlp], axis=0)
    return grad_x, (d_norms, d_conv, sums, d_fox_norm, d_gdn_norm, loss_row), reduced_late, reduced_in


def kernel(x, pre_mix_norm, w_in, fox_f_bias, fox_out_norm, gdn_conv_w, gdn_a_log, gdn_dt_bias, gdn_out_norm, w_out, post_mix_norm, pre_mlp_norm, w_up, w_down, post_mlp_norm, loss_target, m_pre_mix_norm, m_w_in, m_fox_f_bias, m_fox_out_norm, m_gdn_conv_w, m_gdn_a_log, m_gdn_dt_bias, m_gdn_out_norm, m_w_out, m_post_mix_norm, m_pre_mlp_norm, m_w_up, m_w_down, m_post_mlp_norm, v_pre_mix_norm, v_w_in, v_fox_f_bias, v_fox_out_norm, v_gdn_conv_w, v_gdn_a_log, v_gdn_dt_bias, v_gdn_out_norm, v_w_out, v_post_mix_norm, v_pre_mlp_norm, v_w_up, v_w_down, v_post_mlp_norm):
    weights = dict(pre_mix_norm=pre_mix_norm, w_in=w_in, fox_f_bias=fox_f_bias, fox_out_norm=fox_out_norm, gdn_conv_w=gdn_conv_w,
                   gdn_a_log=gdn_a_log, gdn_dt_bias=gdn_dt_bias, gdn_out_norm=gdn_out_norm, w_out=w_out, post_mix_norm=post_mix_norm,
                   pre_mlp_norm=pre_mlp_norm, w_up=w_up, w_down=w_down, post_mlp_norm=post_mlp_norm)
    m_in = dict(pre_mix_norm=m_pre_mix_norm, w_in=m_w_in, fox_f_bias=m_fox_f_bias, fox_out_norm=m_fox_out_norm, gdn_conv_w=m_gdn_conv_w,
                gdn_a_log=m_gdn_a_log, gdn_dt_bias=m_gdn_dt_bias, gdn_out_norm=m_gdn_out_norm, w_out=m_w_out, post_mix_norm=m_post_mix_norm,
                pre_mlp_norm=m_pre_mlp_norm, w_up=m_w_up, w_down=m_w_down, post_mlp_norm=m_post_mlp_norm)
    v_in = dict(pre_mix_norm=v_pre_mix_norm, w_in=v_w_in, fox_f_bias=v_fox_f_bias, fox_out_norm=v_fox_out_norm, gdn_conv_w=v_gdn_conv_w,
                gdn_a_log=v_gdn_a_log, gdn_dt_bias=v_gdn_dt_bias, gdn_out_norm=v_gdn_out_norm, w_out=v_w_out, post_mix_norm=v_post_mix_norm,
                pre_mlp_norm=v_pre_mlp_norm, w_up=v_w_up, w_down=v_w_down, post_mlp_norm=v_post_mlp_norm)
    order_w = ("pre_mix_norm", "w_in", "fox_f_bias", "fox_out_norm", "gdn_conv_w", "gdn_a_log", "gdn_dt_bias", "gdn_out_norm", "w_out",
               "post_mix_norm", "pre_mlp_norm", "w_up", "w_down", "post_mlp_norm")
    big = ("w_in", "w_out", "w_up", "w_down")

    def row(v):
        return v if v.ndim == 2 else v.reshape(1, -1)

    win_shard = jnp.pad(w_in.T.astype(BF16), ((0, D - CW), (0, 0)))

    def resolve_first(gathered):
        win_g, conv_g = gathered
        return (_to_padded_rows(_with_own(win_g, win_shard)),
                _with_own(conv_g, gdn_conv_w).transpose(1, 0, 2).reshape(CONV_K, 3 * DGDN))

    late_shards = [weights[n].astype(BF16) for n in big[1:]]

    gathered_down = []

    def resolve_out_up(gathered_out, gathered_mlp):
        gathered_down.append(gathered_mlp[1])
        return _with_own(gathered_out[0], late_shards[0]).reshape(D, D), _with_own(gathered_mlp[0], late_shards[1])

    def resolve_down(_):
        return _with_own(gathered_down[0], late_shards[2]).reshape(DFF, D)

    pair_sums, late_blocks = {}, []

    def pair_summed(names, blocks, theirs):
        for n, s in zip(names, _pair_sum(blocks, theirs, "grads_pair_sum_" + names[0])):
            pair_sums[n] = s

    def late_pair_exchange(dwout, dwup3, dwdown):
        late_blocks.extend([dwout.reshape(NCHIP, D // NCHIP, D), dwup3, dwdown.reshape(NCHIP, DFF // NCHIP, D)])
        return _pair_exchange(late_blocks)

    def late_chip_exchange(theirs):
        pair_summed(big[1:], late_blocks, theirs)
        return _chip_exchange([pair_sums["w_up"], pair_sums["w_down"]])

    def reduce_in(dwin_p):
        pair_sums["w_in"] = _from_padded_rows_pair_sum(dwin_p)
        return _chip_exchange([pair_sums["w_in"]], rows_by_core=[(D // 2, CW_SENT - D // 2)])

    grad_x, small, received_late, received_in = _local_step(
        x[0], loss_target[0], (_allgather_exchange([win_shard], whole=[gdn_conv_w], sent_rows=[CW_SENT]), resolve_first),
        (_allgather_exchange(late_shards[:1]), _allgather_exchange(late_shards[1:]), None, resolve_out_up, resolve_down),
        (late_pair_exchange, late_chip_exchange, lambda: None, lambda: _chip_exchange([pair_sums["w_out"]])),
        reduce_in, row(pre_mix_norm), fox_f_bias, row(fox_out_norm), gdn_a_log, gdn_dt_bias,
        row(gdn_out_norm), row(post_mix_norm), row(pre_mlp_norm), row(post_mlp_norm))
    received_mlp, _, received_out = received_late

    g_mine, g_theirs, small_gathered = _chip_sum(
        [pair_sums[n] for n in big], list(received_in[:1]) + list(received_out[:1]) + list(received_mlp[:2]),
        exchange=_small_gather(*small))

    g_big, d_big, nm_big, nv_big = _adamw_big(
        [weights[n] for n in big[1:]], g_mine[1:], g_theirs[1:], [m_in[n] for n in big[1:]], [v_in[n] for n in big[1:]])
    in_t = _adamw_in(w_in.T, g_mine[0], g_theirs[0], m_w_in.T, v_w_in.T)
    g_small, d_small, nm_small, nv_small, loss_total = _small_adamw(
        small_gathered, [row(weights[n]) for n in SMALL_NAMES], [row(m_in[n]) for n in SMALL_NAMES],
        [row(v_in[n]) for n in SMALL_NAMES])

    grads, delta, new_m, new_v = {}, {}, {}, {}
    grads["w_in"], delta["w_in"], new_m["w_in"], new_v["w_in"] = [t.T for t in in_t]
    for i, n in enumerate(big[1:]):
        grads[n], delta[n], new_m[n], new_v[n] = g_big[i], d_big[i], nm_big[i], nv_big[i]
    for i, n in enumerate(SMALL_NAMES):
        shape = weights[n].shape
        grads[n], delta[n], new_m[n], new_v[n] = (g_small[i].reshape(shape), d_small[i].reshape(shape),
                                                  nm_small[i].reshape(shape), nv_small[i].reshape(shape))
    return (loss_total[0, 0], grad_x[None], *[grads[n] for n in order_w], *[delta[n] for n in order_w], *[new_m[n] for n in order_w],
            *[new_v[n] for n in order_w])
```

```python
import jax
import jax.numpy as jnp
from jax import lax
from jax.experimental import pallas as pl
from jax.experimental.pallas import tpu as pltpu

F32 = jnp.float32
BF16 = jnp.bfloat16
MESH = pl.DeviceIdType.MESH

S = 2048
D = 1024
NFH, FHD = 8, 64
NPAIR = NFH // 2
NGH, GHD = 4, 128
DFOX = NFH * FHD
DGDN = NGH * GHD
CHUNK = 64
NCH = S // CHUNK
CONV_K = 4
DFF = 4 * D
EPS = 1e-6
DPROJ = 3600
LANES = 128
DPROJ_PAD = 3840
BLK_GDN = 12
BLK_GZ = 24
BLK_SMALL = 28
NCHIP = 4
NDEV = 8
VMEM_LIMIT = 56 * 1024 * 1024

ADAM_LR = 0.001
ADAM_B1 = 0.9
ADAM_B2 = 0.999
ADAM_EPS = 1e-08
ADAM_WD = 0.01
ADAM_STEP = 10


def _cparams(**kw):
    return pltpu.CompilerParams(vmem_limit_bytes=VMEM_LIMIT, **kw)


def _dn(ca, cb):
    return (((ca,), (cb,)), ((), ()))


def _dot(a, b, ca=1, cb=0):
    return lax.dot_general(a.astype(BF16), b.astype(BF16), _dn(ca, cb), preferred_element_type=F32)


def _hdot(a, b, ca=1, cb=0):
    return lax.dot_general(a.astype(F32), b.astype(F32), _dn(ca, cb), precision=lax.Precision.HIGHEST,
                           preferred_element_type=F32)


def _dot3(a, b, ca=1, cb=0):
    a_hi, b_hi = a.astype(BF16), b.astype(BF16)
    a_lo, b_lo = (a - a_hi.astype(F32)).astype(BF16), (b - b_hi.astype(F32)).astype(BF16)
    dn = _dn(ca, cb)
    return (lax.dot_general(a_hi, b_hi, dn, preferred_element_type=F32)
            + (lax.dot_general(a_hi, b_lo, dn, preferred_element_type=F32)
               + lax.dot_general(a_lo, b_hi, dn, preferred_element_type=F32)))


@jax.custom_vjp
def _mm_nn(a, b):
    return _dot(a, b, 1, 0)


def _mm_nn_fwd(a, b):
    return _dot(a, b, 1, 0), (a, b)


def _mm_nn_bwd(res, g):
    a, b = res
    return _dot(g, b, 1, 1), _dot(a, g, 0, 0)


_mm_nn.defvjp(_mm_nn_fwd, _mm_nn_bwd)


@jax.custom_vjp
def _mm_nt(a, b):
    return _dot(a, b, 1, 1)


def _mm_nt_fwd(a, b):
    return _dot(a, b, 1, 1), (a, b)


def _mm_nt_bwd(res, g):
    a, b = res
    return _dot(g, b, 1, 0), _dot(g, a, 0, 0)


_mm_nt.defvjp(_mm_nt_fwd, _mm_nt_bwd)


@jax.custom_vjp
def _saved_inverse(m, t_inv):
    del m
    return t_inv


def _saved_inverse_fwd(m, t_inv):
    del m
    return t_inv, t_inv


def _saved_inverse_bwd(t_inv, g):
    return -_dot3(_dot3(t_inv, g, 0, 0), t_inv, 1, 1), jnp.zeros_like(t_inv)


_saved_inverse.defvjp(_saved_inverse_fwd, _saved_inverse_bwd)


def _sigmoid(z):
    return 1.0 / (1.0 + jnp.exp(-z))


def _softplus(z):
    return jnp.maximum(z, 0.0) + jnp.log(1.0 + jnp.exp(-jnp.abs(z)))


def _silu(z):
    return z * _sigmoid(z)


def _rms_scale(x):
    return lax.rsqrt(jnp.mean(x * x, axis=-1, keepdims=True) + EPS)


def _rms_bwd(x, w, g):
    r = _rms_scale(x)
    gw = g * w
    dx = r * gw - x * (r * r * r) * jnp.mean(gw * x, axis=-1, keepdims=True)
    return dx, g * x * r


def _matmul(a, b, *, name, ta=False, tb=False, tm=512, tn=512, tk=512, out_dtypes=(F32,), b3=False, o3=False,
            extra=(), epilogue=None, exchange=None, n_sums=0):
    m, k = (a.shape[1], a.shape[0]) if ta else a.shape
    if b3:
        n = b.shape[1] if tb else b.shape[0] * b.shape[2]
        kb = b.shape[0] * b.shape[2] if tb else b.shape[1]
    else:
        n, kb = (b.shape[0], b.shape[1]) if tb else (b.shape[1], b.shape[0])
    assert kb == k, (name, kb, k)
    tm, tn, tk = min(tm, m), min(tn, n), min(tk, k)
    assert m % tm == 0 and n % tn == 0 and k % tk == 0, (name, m, n, k, tm, tn, tk)
    nk = k // tk
    whole_k_blocks = b3 and tb and not ta and nk == 1 and b.shape[0] > 1
    n_extra = len(extra)
    n_out = len(out_dtypes)
    grid = (m // tm, n // tn, nk)
    ex_in, ex_in_specs, ex_out_specs, ex_out_shape, ex_scratch = _hosted(exchange)

    def body(*refs):
        a_ref, b_ref = refs[0], refs[1]
        extra_refs = refs[2:2 + n_extra]
        first_out = 2 + n_extra + len(ex_in)
        out_refs = refs[first_out:first_out + n_out]
        ex_refs = refs[2 + n_extra:first_out] + refs[first_out + n_out:first_out + n_out + len(ex_out_shape)] + refs[-2:]
        step = [pl.program_id(d) for d in range(3)]

        if exchange is not None:
            @pl.when((step[0] == 0) & (step[1] == 0) & (step[2] == 0))
            def _():
                exchange.start(*exchange.split(ex_refs))

        def finish(acc):
            outs = (acc,) if epilogue is None else epilogue(acc, *[r[...] for r in extra_refs])
            for o_ref, val in zip(out_refs[:n_out - n_sums], outs):
                o_ref[...] = val.astype(o_ref.dtype)
            for o_ref, val in zip(out_refs[n_out - n_sums:], outs[n_out - n_sums:]):
                @pl.when(step[0] == 0)
                def _(o_ref=o_ref, val=val):
                    o_ref[...] = val

                @pl.when(step[0] > 0)
                def _(o_ref=o_ref, val=val):
                    o_ref[...] += val

        if whole_k_blocks:
            width = b.shape[2]
            part = _dot(a_ref[:, 0:width], b_ref[0], 1, 1)
            for blk in range(1, b.shape[0]):
                part = part + _dot(a_ref[:, blk * width:(blk + 1) * width], b_ref[blk], 1, 1)
        else:
            part = _dot(a_ref[...], b_ref[...], 0 if ta else 1, 1 if tb else 0)
        if nk == 1:
            finish(part)
        else:
            acc_ref = refs[first_out + n_out + len(ex_out_shape)]

            @pl.when(step[2] == 0)
            def _():
                acc_ref[...] = part

            @pl.when(step[2] > 0)
            def _():
                acc_ref[...] += part

            @pl.when(step[2] == nk - 1)
            def _():
                finish(acc_ref[...])

        if exchange is not None:
            flat = (step[0] * grid[1] + step[1]) * nk + step[2]
            total = grid[0] * grid[1] * nk

            @pl.when(flat == total // 2)
            def _():
                exchange.middle(*exchange.split(ex_refs))

            @pl.when(flat == total - 1)
            def _():
                exchange.rest(*exchange.split(ex_refs))

    a_spec = pl.BlockSpec((tk, tm), lambda i, j, kk: (kk, i)) if ta else pl.BlockSpec((tm, tk), lambda i, j, kk: (i, kk))
    if whole_k_blocks:
        b_spec = pl.BlockSpec((b.shape[0], tn, b.shape[2]), lambda i, j, kk: (0, j, 0))
    elif b3 and tb:
        assert b.shape[2] == tk
        b_spec = pl.BlockSpec((None, tn, tk), lambda i, j, kk: (kk, j, 0))
    elif b3:
        assert b.shape[2] == tn
        b_spec = pl.BlockSpec((None, tk, tn), lambda i, j, kk: (j, kk, 0))
    elif tb:
        b_spec = pl.BlockSpec((tn, tk), lambda i, j, kk: (j, kk))
    else:
        b_spec = pl.BlockSpec((tk, tn), lambda i, j, kk: (kk, j))
    tile = pl.BlockSpec((tm, tn), lambda i, j, kk: (i, j))
    out_specs = [tile] * n_out
    out_shape = [jax.ShapeDtypeStruct((m, n), dt) for dt in out_dtypes]
    if o3:
        out_specs[0] = pl.BlockSpec((None, tm, tn), lambda i, j, kk: (j, i, 0))
        out_shape[0] = jax.ShapeDtypeStruct((n // tn, m, tn), out_dtypes[0])
    assert n_sums == 0 or tn == n
    for r in range(n_out - n_sums, n_out):
        out_specs[r] = pl.BlockSpec((1, tn), lambda i, j, kk: (0, 0))
        out_shape[r] = jax.ShapeDtypeStruct((1, n), out_dtypes[r])
    res = pl.pallas_call(
        body, name=name, grid=grid,
        in_specs=[a_spec, b_spec] + [tile if e.shape[0] == m else pl.BlockSpec((1, tn), lambda i, j, kk: (0, j)) for e in extra]
        + ex_in_specs, out_specs=out_specs + ex_out_specs,
        out_shape=out_shape + ex_out_shape,
        scratch_shapes=([pltpu.VMEM((tm, tn), F32)] if nk > 1 else []) + ex_scratch,
        compiler_params=_cparams(),
    )(a, b, *extra, *ex_in)
    if exchange is not None:
        return (res[0] if n_out == 1 else res[:n_out]), res[n_out:]
    return res[0] if n_out == 1 else res


TR = 512


def _row_spec(cols):
    return pl.BlockSpec((TR, cols), lambda i: (i, 0))


def _vec_spec(cols):
    return pl.BlockSpec((1, cols), lambda i: (0, 0))


def _pre_norm(x, w, exchange=None):
    ex_in, ex_in_specs, ex_out_specs, ex_out_shape, ex_scratch = _hosted(exchange)

    def body(*refs):
        x_ref, w_ref, h_ref = refs[0], refs[1], refs[2 + len(ex_in)]
        ex_refs = refs[2:2 + len(ex_in)] + refs[3 + len(ex_in):]
        if exchange is not None:
            @pl.when(pl.program_id(0) == 0)
            def _():
                exchange.start(*exchange.split(ex_refs))

        xv = x_ref[...]
        h_ref[...] = (xv * _rms_scale(xv) * w_ref[...]).astype(BF16)

        if exchange is not None:
            @pl.when(pl.program_id(0) == S // TR - 1)
            def _():
                exchange.finish(*exchange.split(ex_refs))

    res = pl.pallas_call(
        body, name="pre_norm", grid=(S // TR,), in_specs=[_row_spec(D), _vec_spec(D)] + ex_in_specs,
        out_specs=[_row_spec(D)] + ex_out_specs, out_shape=[jax.ShapeDtypeStruct((S, D), BF16)] + ex_out_shape,
        scratch_shapes=ex_scratch, compiler_params=_cparams(),
    )(x, w, *ex_in)
    return res[0], res[1:]


def _pre_norm_bwd(dh, x, w, dx1):
    def body(dh_ref, x_ref, w_ref, dx1_ref, dx_ref, dw_ref):
        i = pl.program_id(0)
        dxa, dwt = _rms_bwd(x_ref[...], w_ref[...], dh_ref[...])
        dx_ref[...] = dx1_ref[...] + dxa

        @pl.when(i == 0)
        def _():
            dw_ref[...] = jnp.zeros_like(dw_ref)

        dw_ref[...] += jnp.sum(dwt, axis=0, keepdims=True)

    return pl.pallas_call(
        body, name="pre_norm_bwd", grid=(S // TR,),
        in_specs=[_row_spec(D), _row_spec(D), _vec_spec(D), _row_spec(D)], out_specs=[_row_spec(D), _vec_spec(D)],
        out_shape=[jax.ShapeDtypeStruct((S, D), F32), jax.ShapeDtypeStruct((1, D), F32)], compiler_params=_cparams(),
    )(dh, x, w, dx1)


BQ = 512
NQ = S // BQ
LANE_BETA, LANE_G = 8, 12


def _gate_lanes(shape):
    lane = lax.broadcasted_iota(jnp.int32, shape, 1)
    return lane < LANE_BETA, (lane >= LANE_BETA) & (lane < LANE_G), (lane >= LANE_G) & (lane < LANE_G + NGH)


def _gates(proj, bias_vec, alog_vec):
    def body(s_ref, b_ref, a_ref, o_ref, carry_ref):
        i = pl.program_id(0)

        @pl.when(i == 0)
        def _():
            carry_ref[...] = jnp.zeros_like(carry_ref)

        z = s_ref[...] + b_ref[...]
        tail = jnp.log(1.0 + jnp.exp(-jnp.abs(z)))
        sp = jnp.maximum(z, 0.0) + tail
        lf = jnp.minimum(z, 0.0) - tail
        r = lax.broadcasted_iota(jnp.int32, (BQ, BQ), 0)
        c = lax.broadcasted_iota(jnp.int32, (BQ, BQ), 1)
        tri = (c <= r).astype(F32)
        cum = _hdot(tri, lf) + carry_ref[...]
        carry_ref[...] = cum[BQ - 1:BQ, :]
        is_fox, is_beta, is_g = _gate_lanes(z.shape)
        o_ref[...] = jnp.where(is_fox, cum, jnp.where(is_beta, _sigmoid(z), jnp.where(is_g, -jnp.exp(a_ref[...]) * sp, 0.0)))

    return pl.pallas_call(
        body, name="gates", grid=(NQ,),
        in_specs=[pl.BlockSpec((BQ, LANES), lambda i: (i, BLK_SMALL)), _vec_spec(LANES), _vec_spec(LANES)],
        out_specs=pl.BlockSpec((BQ, LANES), lambda i: (i, 0)), out_shape=jax.ShapeDtypeStruct((S, LANES), F32),
        scratch_shapes=[pltpu.VMEM((1, LANES), F32)], compiler_params=_cparams(),
    )(proj, bias_vec, alog_vec)


def _gates_bwd(proj, bias_vec, alog_vec, dgates_gdn, dcum_fox, dproj):
    def body(s_ref, b_ref, a_ref, dg_ref, dc_ref, dproj_in, dproj_ref, red_ref, carry_ref):
        del dproj_in
        i = pl.program_id(0)

        @pl.when(i == 0)
        def _():
            carry_ref[...] = jnp.zeros_like(carry_ref)
            red_ref[...] = jnp.zeros_like(red_ref)

        z = s_ref[...] + b_ref[...]
        dg = dg_ref[...] + dc_ref[...]
        r = lax.broadcasted_iota(jnp.int32, (BQ, BQ), 0)
        c = lax.broadcasted_iota(jnp.int32, (BQ, BQ), 1)
        upper = (c >= r).astype(F32)
        dlf = _hdot(upper, dg) + carry_ref[...]
        carry_ref[...] = dlf[0:1, :]
        sig = _sigmoid(z)
        g_scale = -jnp.exp(a_ref[...])
        is_fox, is_beta, is_g = _gate_lanes(z.shape)
        ds = jnp.where(is_fox, dlf * (1.0 - sig), jnp.where(is_beta, dg * sig * (1.0 - sig), jnp.where(is_g, dg * g_scale * sig, 0.0)))
        dproj_ref[:, 0:LANES] = ds.astype(BF16)
        dproj_ref[:, LANES:2 * LANES] = jnp.zeros((BQ, LANES), BF16)
        dalog = jnp.where(is_g, dg * g_scale * _softplus(z), 0.0)
        sums = jnp.sum(ds, axis=0, keepdims=True)
        red_ref[0:1, :] += jnp.where(is_fox[0:1], sums, 0.0)
        red_ref[1:2, :] += pltpu.roll(jnp.where(is_g[0:1], sums, 0.0), LANES - LANE_G, 1)
        red_ref[2:3, :] += pltpu.roll(jnp.sum(dalog, axis=0, keepdims=True), LANES - LANE_G, 1)

    blk = pl.BlockSpec((BQ, LANES), lambda i: (NQ - 1 - i, 0))
    return pl.pallas_call(
        body, name="gates_bwd", grid=(NQ,),
        in_specs=[pl.BlockSpec((BQ, LANES), lambda i: (NQ - 1 - i, BLK_SMALL)), _vec_spec(LANES), _vec_spec(LANES), blk, blk,
                  pl.BlockSpec(memory_space=pl.ANY)],
        out_specs=[pl.BlockSpec((BQ, 2 * LANES), lambda i: (NQ - 1 - i, BLK_SMALL // 2)), pl.BlockSpec((8, LANES), lambda i: (0, 0))],
        out_shape=[jax.ShapeDtypeStruct((S, DPROJ_PAD), BF16), jax.ShapeDtypeStruct((8, LANES), F32)],
        input_output_aliases={5: 0},
        scratch_shapes=[pltpu.VMEM((1, LANES), F32)], compiler_params=_cparams(),
    )(proj, bias_vec, alog_vec, dgates_gdn, dcum_fox, dproj)


FOX_SCALE = FHD ** -0.5
FOX_PAIRS = 2
FOX_PAIRS_BWD = 2


def _head_mask(e):
    lane = lax.broadcasted_iota(jnp.int32, (1, LANES), 1)
    return (lane >= e * FHD) & (lane < (e + 1) * FHD)


def _lane_col(vals, index):
    lane = lax.broadcasted_iota(jnp.int32, vals.shape, 1)
    return jnp.sum(jnp.where(lane == index, vals, 0.0), axis=1, keepdims=True)


def _sublane_row(vals, index):
    row = lax.broadcasted_iota(jnp.int32, vals.shape, 0)
    return jnp.sum(jnp.where(row == index, vals, 0.0), axis=0, keepdims=True)


def _pair_cols(c0, c1):
    lane = lax.broadcasted_iota(jnp.int32, (c0.shape[0], 2), 1)
    return jnp.where(lane == 0, c0, c1)


def _split3(x):
    hi = x.astype(BF16).astype(F32)
    rest = x - hi
    mid = rest.astype(BF16).astype(F32)
    return hi, mid, (rest - mid).astype(BF16).astype(F32)


def _fox_operand(vals, e, cum, is_query):
    lane = lax.broadcasted_iota(jnp.int32, (1, LANES), 1)
    base = (1 - e) * FHD
    parts = _split3(cum)
    own = jnp.where(_head_mask(e), vals * FOX_SCALE if is_query else vals, 0.0)
    cum_at, ones_at = (base, base + 3) if is_query else (base + 3, base)
    sign = 1.0 if is_query else -1.0
    out = own + jnp.where((lane >= ones_at) & (lane < ones_at + 3), 1.0, 0.0)
    for i, part in enumerate(parts):
        out = out + jnp.where(lane == cum_at + i, sign * part, 0.0)
    return out.astype(BF16)


def _causal_block():
    return lax.broadcasted_iota(jnp.int32, (BQ, BQ), 1) <= lax.broadcasted_iota(jnp.int32, (BQ, BQ), 0)


def _head_rms(o, masks):
    o2 = o * o
    r = [lax.rsqrt(jnp.sum(jnp.where(mk, o2, 0.0), axis=1, keepdims=True) * (1.0 / FHD) + EPS) for mk in masks]
    return jnp.where(masks[0], r[0], r[1])


def _hosted(exchange):
    if exchange is None:
        return [], [], [], [], []
    return (exchange.inputs, [HBM] * len(exchange.inputs), [HBM] * len(exchange.out_shape), exchange.out_shape,
            exchange.sem_shapes())


def _fox_fwd(proj, gates, w2, exchange=None):
    ex_in, ex_in_specs, ex_out_specs, ex_out_shape, ex_scratch = _hosted(exchange)

    n_in = 3 * FOX_PAIRS + 2
    heads = [(pp, e) for pp in range(FOX_PAIRS) for e in range(2)]

    def body(*refs):
        qkv_refs, g_ref, w_ref = refs[:3 * FOX_PAIRS], refs[3 * FOX_PAIRS], refs[3 * FOX_PAIRS + 1]
        mix_ref, o_ref, lse_ref = refs[n_in + len(ex_in):n_in + 3 + len(ex_in)]
        ka_ref, vb_ref = refs[n_in + 3 + len(ex_in) + len(ex_out_shape):n_in + 5 + len(ex_in) + len(ex_out_shape)]
        ex_refs = refs[n_in:n_in + len(ex_in)] + refs[n_in + 3 + len(ex_in):n_in + 3 + len(ex_in) + len(ex_out_shape)] + refs[-2:]
        grp, qi = pl.program_id(0), pl.program_id(1)

        def head_index(pp, e):
            return 2 * (FOX_PAIRS * grp + pp) + e

        if exchange is not None:
            @pl.when((grp == 0) & (qi == 0))
            def _():
                exchange.start(*exchange.split(ex_refs))

        @pl.when(qi == 0)
        def _():
            gt = g_ref[...]
            for pp in range(FOX_PAIRS):
                kv = qkv_refs[3 * pp + 1][...]
                for e in range(2):
                    ka_ref[2 * pp + e] = _fox_operand(kv, e, _lane_col(gt, head_index(pp, e)), False)
                vb_ref[pp] = qkv_refs[3 * pp + 2][...].astype(BF16)

        masks = [_head_mask(0), _head_mask(1)]
        gt = g_ref[pl.ds(pl.multiple_of(qi * BQ, BQ), BQ), :]
        qs = [_fox_operand(qkv_refs[3 * pp][...], e, _lane_col(gt, head_index(pp, e)), True) for pp, e in heads]
        n = range(len(heads))

        def block(kj, carry, diagonal):
            rows = pl.ds(pl.multiple_of(kj * BQ, BQ), BQ)
            s = [_dot(qs[i], ka_ref[i, rows, :], 1, 1) for i in n]
            if diagonal:
                s = [jnp.where(_causal_block(), s[i], -jnp.inf) for i in n]
            m_new = [jnp.maximum(carry[i][0], jnp.max(s[i], axis=-1, keepdims=True)) for i in n]
            p = [jnp.exp(s[i] - m_new[i]) for i in n]
            alpha = [jnp.exp(carry[i][0] - m_new[i]) for i in n]
            l_new = [alpha[i] * carry[i][1] + jnp.sum(p[i], axis=-1, keepdims=True) for i in n]
            pv = [_dot(p[i], vb_ref[heads[i][0], rows, :]) for i in n]
            return tuple((m_new[i], l_new[i], alpha[i] * carry[i][2] + pv[i]) for i in n)

        one = (jnp.full((BQ, 1), -jnp.inf, F32), jnp.zeros((BQ, 1), F32), jnp.zeros((BQ, LANES), F32))
        below = lax.fori_loop(0, qi, lambda kj, carry: block(kj, carry, False), (one,) * len(heads))
        done = block(qi, below, True)
        for pp in range(FOX_PAIRS):
            (m0, l0, a0), (m1, l1, a1) = done[2 * pp], done[2 * pp + 1]
            o = jnp.where(masks[0], a0 / l0, a1 / l1)
            cols = slice(pp * LANES, (pp + 1) * LANES)
            o_ref[:, cols] = o
            mix_ref[:, cols] = (o * _head_rms(o, masks) * w_ref[...]).astype(BF16)
            lse_ref[pp] = _pair_cols(m0 + jnp.log(l0), m1 + jnp.log(l1))

        if exchange is not None:
            @pl.when((grp == NPAIR // FOX_PAIRS // 2) & (qi == 0))
            def _():
                exchange.middle(*exchange.split(ex_refs))

            @pl.when((grp == NPAIR // FOX_PAIRS - 1) & (qi == NQ - 1))
            def _():
                exchange.rest(*exchange.split(ex_refs))

    qkv_specs = []
    for pp in range(FOX_PAIRS):
        qkv_specs.append(pl.BlockSpec((BQ, LANES), lambda g, i, pp=pp: (i, 3 * (FOX_PAIRS * g + pp))))
        qkv_specs.append(pl.BlockSpec((S, LANES), lambda g, i, pp=pp: (0, 3 * (FOX_PAIRS * g + pp) + 1)))
        qkv_specs.append(pl.BlockSpec((S, LANES), lambda g, i, pp=pp: (0, 3 * (FOX_PAIRS * g + pp) + 2)))
    blk = pl.BlockSpec((BQ, FOX_PAIRS * LANES), lambda g, i: (i, g))
    res = pl.pallas_call(
        body, name="fox_fwd", grid=(NPAIR // FOX_PAIRS, NQ),
        in_specs=qkv_specs + [pl.BlockSpec((S, LANES), lambda g, i: (0, 0)), pl.BlockSpec((1, LANES), lambda g, i: (0, 0))]
        + ex_in_specs,
        out_specs=[blk, blk, pl.BlockSpec((FOX_PAIRS, BQ, 2), lambda g, i: (g, i, 0))] + ex_out_specs,
        out_shape=[jax.ShapeDtypeStruct((S, D), BF16), jax.ShapeDtypeStruct((S, DFOX), F32),
                   jax.ShapeDtypeStruct((NPAIR, S, 2), F32)] + ex_out_shape,
        scratch_shapes=[pltpu.VMEM((2 * FOX_PAIRS, S, LANES), BF16), pltpu.VMEM((FOX_PAIRS, S, LANES), BF16)] + ex_scratch,
        compiler_params=_cparams(),
    )(*([proj] * (3 * FOX_PAIRS)), gates, w2, *ex_in)
    return res[0], res[1], res[2], res[3:]


def _fox_norm_bwd(o, dmix, w2, exchange=None):
    ex_in, ex_in_specs, ex_out_specs, ex_out_shape, ex_scratch = _hosted(exchange)

    def body(*refs):
        o_ref, g_ref, w_ref = refs[:3]
        do_ref, dl_ref, dw_ref = refs[3 + len(ex_in):6 + len(ex_in)]
        ex_refs = refs[3:3 + len(ex_in)] + refs[6 + len(ex_in):]
        hp, qi = pl.program_id(0), pl.program_id(1)

        if exchange is not None:
            @pl.when((hp == 0) & (qi == 0))
            def _():
                exchange.start(*exchange.split(ex_refs))

        masks = [_head_mask(0), _head_mask(1)]
        ov = o_ref[...]
        g = g_ref[...]
        r = _head_rms(ov, masks)
        gw = g * w_ref[...]
        gwo = gw * ov
        mean = [jnp.sum(jnp.where(mk, gwo, 0.0), axis=1, keepdims=True) * (1.0 / FHD) for mk in masks]
        do = r * gw - ov * (r * r * r) * jnp.where(masks[0], mean[0], mean[1])
        do_ref[...] = do.astype(BF16)
        doo = do * ov
        dl_ref[...] = _pair_cols(*[jnp.sum(jnp.where(mk, doo, 0.0), axis=1, keepdims=True) for mk in masks])

        @pl.when((hp == 0) & (qi == 0))
        def _():
            dw_ref[...] = jnp.zeros_like(dw_ref)

        dw_ref[...] += jnp.sum(g * ov * r, axis=0, keepdims=True)

        @pl.when((hp == NPAIR - 1) & (qi == NQ - 1))
        def _():
            dw = dw_ref[...]
            dw_ref[...] = dw + pltpu.roll(dw, FHD, 1)
            if exchange is not None:
                exchange.finish(*exchange.split(ex_refs))

    blk = pl.BlockSpec((BQ, LANES), lambda hp, i: (i, hp))
    vec = pl.BlockSpec((1, LANES), lambda hp, i: (0, 0))
    res = pl.pallas_call(
        body, name="fox_norm_bwd", grid=(NPAIR, NQ), in_specs=[blk, blk, vec] + ex_in_specs,
        out_specs=[blk, pl.BlockSpec((None, BQ, 2), lambda hp, i: (hp, i, 0)), vec] + ex_out_specs,
        out_shape=[jax.ShapeDtypeStruct((S, DFOX), BF16), jax.ShapeDtypeStruct((NPAIR, S, 2), F32),
                   jax.ShapeDtypeStruct((1, LANES), F32)] + ex_out_shape,
        scratch_shapes=ex_scratch, compiler_params=_cparams(),
    )(o, dmix, w2, *ex_in)
    return res[0], res[1], res[2], res[3:]


def _fox_bwd(proj, do, gates, lse, delta, exchange=None):
    ex_in, ex_in_specs, ex_out_specs, ex_out_shape, ex_scratch = _hosted(exchange)

    pg = FOX_PAIRS_BWD
    n_in = 3 * pg + 4
    heads = [(pp, e) for pp in range(pg) for e in range(2)]

    def body(*refs):
        qkv_refs = refs[:3 * pg]
        do_ref, g_ref, lse_ref, dl_ref = refs[3 * pg:n_in]
        dproj_ref, dc_ref = refs[n_in + len(ex_in):n_in + 2 + len(ex_in)]
        qa_ref, dq_ref = refs[n_in + 2 + len(ex_in) + len(ex_out_shape):n_in + 4 + len(ex_in) + len(ex_out_shape)]
        ex_refs = refs[n_in:n_in + len(ex_in)] + refs[n_in + 2 + len(ex_in):n_in + 2 + len(ex_in) + len(ex_out_shape)] + refs[-2:]
        grp, kj = pl.program_id(0), pl.program_id(1)

        def head_index(pp, e):
            return 2 * (pg * grp + pp) + e

        if exchange is not None:
            @pl.when((grp == 0) & (kj == 0))
            def _():
                exchange.start(*exchange.split(ex_refs))

        @pl.when(kj == 0)
        def _():
            gt = g_ref[...]
            for pp in range(pg):
                qv = qkv_refs[3 * pp][...]
                for e in range(2):
                    qa_ref[2 * pp + e] = _fox_operand(qv, e, _lane_col(gt, head_index(pp, e)), True)
            dq_ref[...] = jnp.zeros_like(dq_ref)

        @pl.when((grp == 0) & (kj == 0))
        def _():
            dc_ref[...] = jnp.zeros_like(dc_ref)

        masks = [_head_mask(0), _head_mask(1)]
        krows = pl.ds(pl.multiple_of(kj * BQ, BQ), BQ)
        gk = g_ref[krows, :]
        kas = [_fox_operand(qkv_refs[3 * pp + 1][...], e, _lane_col(gk, head_index(pp, e)), False) for pp, e in heads]
        vbs = [qkv_refs[3 * pp + 2][...].astype(BF16) for pp in range(pg)]
        lane = lax.broadcasted_iota(jnp.int32, (BQ, LANES), 1)
        n = range(len(heads))

        def block(qi, carry, diagonal):
            dks, dvs, css = carry
            rows = pl.ds(pl.multiple_of(qi * BQ, BQ), BQ)
            qa = [qa_ref[i, rows, :] for i in n]
            s = [_dot(qa[i], kas[i], 1, 1) for i in n]
            if diagonal:
                s = [jnp.where(_causal_block(), s[i], -jnp.inf) for i in n]
            dov = [do_ref[rows, pp * LANES:(pp + 1) * LANES] for pp in range(pg)]
            doe = [jnp.where(masks[e], dov[pp], jnp.zeros_like(dov[pp])) for pp, e in heads]
            lse2 = [lse_ref[pp, rows, :] for pp in range(pg)]
            dl2 = [dl_ref[pp, rows, :] for pp in range(pg)]
            p = [jnp.exp(s[i] - _lane_col(lse2[heads[i][0]], heads[i][1])) for i in n]
            dp = [_dot(doe[i], vbs[heads[i][0]], 1, 1) for i in n]
            ds = [p[i] * (dp[i] - _lane_col(dl2[heads[i][0]], heads[i][1])) for i in n]
            dv_part = [_dot(p[i], doe[i], 0, 0) for i in n]
            dk_part = [_dot(ds[i], jnp.where(masks[heads[i][1]], qa[i], jnp.zeros_like(qa[i])), 0, 0) for i in n]
            dq_part = [jnp.where(masks[heads[i][1]], _dot(ds[i], kas[i]), 0.0) for i in n]
            css = tuple(css[i] + jnp.sum(ds[i], axis=0, keepdims=True) for i in n)
            dc = jnp.zeros((BQ, LANES), F32)
            for i in n:
                dc = dc + jnp.where(lane == head_index(*heads[i]), jnp.sum(ds[i], axis=1, keepdims=True), 0.0)
            for pp in range(pg):
                dq_ref[pp, rows, :] += (dq_part[2 * pp] + dq_part[2 * pp + 1]) * FOX_SCALE
            dc_ref[rows, :] += dc
            dks = tuple(dks[pp] + dk_part[2 * pp] + dk_part[2 * pp + 1] for pp in range(pg))
            dvs = tuple(dvs[pp] + dv_part[2 * pp] + dv_part[2 * pp + 1] for pp in range(pg))
            return dks, dvs, css

        zero = jnp.zeros((BQ, LANES), F32)
        first = block(kj, ((zero,) * pg, (zero,) * pg, (jnp.zeros((1, BQ), F32),) * len(heads)), True)
        dks, dvs, css = lax.fori_loop(kj + 1, NQ, lambda qi, carry: block(qi, carry, False), first)
        r = lax.broadcasted_iota(jnp.int32, (BQ, BQ), 0)
        c = lax.broadcasted_iota(jnp.int32, (BQ, BQ), 1)
        dcol = jnp.zeros((BQ, LANES), F32)
        for i in n:
            col = jnp.sum(jnp.where(r == c, css[i], 0.0), axis=1, keepdims=True)
            dcol = dcol + jnp.where(lane == head_index(*heads[i]), col, 0.0)
        dc_ref[krows, :] -= dcol
        for pp in range(pg):
            base = 3 * pp * LANES
            dproj_ref[krows, base + LANES:base + 2 * LANES] = dks[pp].astype(BF16)
            dproj_ref[krows, base + 2 * LANES:base + 3 * LANES] = dvs[pp].astype(BF16)

        @pl.when(kj == NQ - 1)
        def _():
            for pp in range(pg):
                dproj_ref[:, 3 * pp * LANES:(3 * pp + 1) * LANES] = dq_ref[pp].astype(BF16)

        if exchange is not None:
            @pl.when((grp == NPAIR // pg // 2) & (kj == 0))
            def _():
                exchange.middle(*exchange.split(ex_refs))

            @pl.when((grp == NPAIR // pg - 1) & (kj == NQ - 1))
            def _():
                exchange.rest(*exchange.split(ex_refs))

    qkv_specs = []
    for pp in range(pg):
        qkv_specs.append(pl.BlockSpec((S, LANES), lambda g, j, pp=pp: (0, 3 * (pg * g + pp))))
        qkv_specs.append(pl.BlockSpec((BQ, LANES), lambda g, j, pp=pp: (j, 3 * (pg * g + pp) + 1)))
        qkv_specs.append(pl.BlockSpec((BQ, LANES), lambda g, j, pp=pp: (j, 3 * (pg * g + pp) + 2)))
    pair = pl.BlockSpec((pg, S, 2), lambda g, j: (g, 0, 0))
    res = pl.pallas_call(
        body, name="fox_bwd", grid=(NPAIR // pg, NQ),
        in_specs=qkv_specs + [pl.BlockSpec((S, pg * LANES), lambda g, j: (0, g)), pl.BlockSpec((S, LANES), lambda g, j: (0, 0)),
                              pair, pair] + ex_in_specs,
        out_specs=[pl.BlockSpec((S, 3 * pg * LANES), lambda g, j: (0, g)), pl.BlockSpec((S, LANES), lambda g, j: (0, 0))]
        + ex_out_specs,
        out_shape=[jax.ShapeDtypeStruct((S, DPROJ_PAD), BF16), jax.ShapeDtypeStruct((S, LANES), F32)] + ex_out_shape,
        scratch_shapes=[pltpu.VMEM((2 * pg, S, LANES), BF16), pltpu.VMEM((pg, S, LANES), F32)] + ex_scratch,
        compiler_params=_cparams(),
    )(*([proj] * (3 * pg)), do, gates, lse, delta, *ex_in)
    return res[0], res[1], res[2:]


NQKV = 3 * NGH
GDN_QSCALE = GHD ** -0.5


def _shift_down(x, s):
    if s == 0:
        return x
    row = lax.broadcasted_iota(jnp.int32, x.shape, 0)
    return jnp.where(row >= s, pltpu.roll(x, s, 0), 0.0)


def _shift_up(x, s):
    if s == 0:
        return x
    n = x.shape[0]
    row = lax.broadcasted_iota(jnp.int32, x.shape, 0)
    return jnp.where(row < n - s, pltpu.roll(x, n - s, 0), 0.0)


def _conv_taps(xv):
    return [_shift_down(xv, CONV_K - 1 - j) for j in range(CONV_K)]


def _conv_pre(taps, wv):
    pre = taps[CONV_K - 1] * wv[CONV_K - 1:CONV_K, :]
    for j in range(CONV_K - 1):
        pre = pre + taps[j] * wv[j:j + 1, :]
    return pre


def _l2_factors(b):
    return b < 2 * NGH, jnp.where(b < NGH, GDN_QSCALE, 1.0)


def _gdn_pre(proj, conv_w):
    def body(x_ref, w_ref, o_ref):
        b = pl.program_id(0)
        c = _silu(_conv_pre(_conv_taps(x_ref[...]), w_ref[...]))
        normed, scale = _l2_factors(b)
        rs = lax.rsqrt(jnp.sum(c * c, axis=-1, keepdims=True) + EPS)
        o_ref[...] = c * jnp.where(normed, rs, 1.0) * scale

    return pl.pallas_call(
        body, name="gdn_pre", grid=(NQKV,),
        in_specs=[pl.BlockSpec((S, GHD), lambda b: (0, BLK_GDN + b)), pl.BlockSpec((CONV_K, GHD), lambda b: (0, b))],
        out_specs=pl.BlockSpec((S, GHD), lambda b: (0, b)),
        out_shape=jax.ShapeDtypeStruct((S, NQKV * GHD), F32), compiler_params=_cparams(),
    )(proj, conv_w)


def _gdn_pre_bwd(proj, conv_w, dqkv, dproj):
    def body(x_ref, w_ref, dy_ref, dproj_in, dx_ref, dw_ref):
        del dproj_in
        b = pl.program_id(0)
        taps = _conv_taps(x_ref[...])
        wv = w_ref[...]
        pre = _conv_pre(taps, wv)
        sig = _sigmoid(pre)
        c = pre * sig
        normed, scale = _l2_factors(b)
        g = dy_ref[...] * scale
        rs = lax.rsqrt(jnp.sum(c * c, axis=-1, keepdims=True) + EPS)
        dc_n = rs * g - c * (rs * rs * rs) * jnp.sum(g * c, axis=-1, keepdims=True)
        dc = jnp.where(normed, dc_n, g)
        dpre = dc * sig * (1.0 + pre * (1.0 - sig))
        dx = dpre * wv[CONV_K - 1:CONV_K, :]
        for j in range(CONV_K - 1):
            dx = dx + _shift_up(dpre, CONV_K - 1 - j) * wv[j:j + 1, :]
        dx_ref[...] = dx.astype(BF16)
        for j in range(CONV_K):
            dw_ref[j:j + 1, :] = jnp.sum(dpre * taps[j], axis=0, keepdims=True)

    return pl.pallas_call(
        body, name="gdn_pre_bwd", grid=(NQKV,),
        in_specs=[pl.BlockSpec((S, GHD), lambda b: (0, BLK_GDN + b)), pl.BlockSpec((CONV_K, GHD), lambda b: (0, b)),
                  pl.BlockSpec((None, S, GHD), lambda b: (b // NGH, 0, b % NGH)), pl.BlockSpec(memory_space=pl.ANY)],
        out_specs=[pl.BlockSpec((S, GHD), lambda b: (0, BLK_GDN + b)), pl.BlockSpec((CONV_K, GHD), lambda b: (0, b))],
        out_shape=[jax.ShapeDtypeStruct((S, DPROJ_PAD), BF16), jax.ShapeDtypeStruct((CONV_K, NQKV * GHD), F32)],
        input_output_aliases={3: 0}, compiler_params=_cparams(),
    )(proj, conv_w, dqkv, dproj)


CB = 16
NCB = NCH // CB


def _chunk_prep(qs, ks, vs, gcols, bcols, t_saved=None):
    n = range(len(qs))
    r = lax.broadcasted_iota(jnp.int32, (CHUNK, CHUNK), 0)
    c = lax.broadcasted_iota(jnp.int32, (CHUNK, CHUNK), 1)
    incl = c <= r
    eye = (r == c).astype(F32)
    grow = [jnp.sum(gcols[i] * eye, axis=0, keepdims=True) for i in n]
    gc_col = [jnp.sum(jnp.where(incl, grow[i], 0.0), axis=1, keepdims=True) for i in n]
    gc_row = [jnp.sum(jnp.where(r <= c, gcols[i], 0.0), axis=0, keepdims=True) for i in n]
    decay = [jnp.exp(jnp.where(incl, gc_col[i] - gc_row[i], -jnp.inf)) for i in n]
    kb = [ks[i] * bcols[i] for i in n]
    vb = [vs[i] * bcols[i] for i in n]
    kk = [_mm_nt(kb[i], ks[i]) for i in n]
    m = [jnp.where(c < r, kk[i] * decay[i], 0.0) for i in n]
    if t_saved is None:
        t_inv = [eye - m[i] for i in n]
        p = [_dot3(m[i], m[i]) for i in n]
        for step in range(5):
            t_inv = [t_inv[i] + _dot3(t_inv[i], p[i]) for i in n]
            if step < 4:
                p = [_dot3(p[i], p[i]) for i in n]
    else:
        t_inv = [_saved_inverse(m[i], t_saved[i]) for i in n]
    egc = [jnp.exp(gc_col[i]) for i in n]
    u = [_mm_nn(t_inv[i], vb[i]) for i in n]
    w = [_mm_nn(t_inv[i], kb[i] * egc[i]) for i in n]
    qk = [_mm_nt(qs[i], ks[i]) for i in n]
    gc_last = [gc_col[i][CHUNK - 1:CHUNK, :] for i in n]
    return [(u[i], w[i], qk[i] * decay[i], qs[i] * egc[i], ks[i] * jnp.exp(gc_last[i] - gc_col[i]), jnp.exp(gc_last[i]),
             t_inv[i]) for i in n]


def _prep_specs():
    rows = CB * CHUNK
    qs = pl.BlockSpec((rows, GHD), lambda i, h: (i, h))
    ks = pl.BlockSpec((rows, GHD), lambda i, h: (i, NGH + h))
    vs = pl.BlockSpec((rows, GHD), lambda i, h: (i, 2 * NGH + h))
    gs = pl.BlockSpec((rows, LANES), lambda i, h: (i, 0))
    a_s = pl.BlockSpec((None, rows, CHUNK), lambda i, h: (h, i, 0))
    gl_s = pl.BlockSpec((None, CB, 1, LANES), lambda i, h: (h, i, 0, 0))
    return qs, ks, vs, gs, a_s, gl_s


def _gdn_prep(qkv, gates, exchange=None):
    ex_in, ex_in_specs, ex_out_specs, ex_out_shape, ex_scratch = _hosted(exchange)

    def body(*refs):
        q_ref, k_ref, v_ref, g_ref = refs[:4]
        u_ref, w_ref, qd_ref, kd_ref, a_ref, gl_ref, t_ref = refs[4 + len(ex_in):11 + len(ex_in)]
        ex_refs = refs[4:4 + len(ex_in)] + refs[11 + len(ex_in):]
        h = pl.program_id(1)

        if exchange is not None:
            @pl.when((pl.program_id(0) == 0) & (h == 0))
            def _():
                exchange.start(*exchange.split(ex_refs))

        chunks = [pl.ds(cidx * CHUNK, CHUNK) for cidx in range(CB)]
        gts = [g_ref[rows, :] for rows in chunks]
        outs = _chunk_prep([q_ref[rows, :] for rows in chunks], [k_ref[rows, :] for rows in chunks],
                           [v_ref[rows, :] for rows in chunks], [_lane_col(gt, LANE_G + h) for gt in gts],
                           [_lane_col(gt, LANE_BETA + h) for gt in gts])
        for cidx, rows in enumerate(chunks):
            u, w, a, qd, kd, gl, t_inv = outs[cidx]
            u_ref[rows, :] = u
            w_ref[rows, :] = w
            qd_ref[rows, :] = qd
            kd_ref[rows, :] = kd
            a_ref[rows, :] = a
            t_ref[rows, :] = t_inv
            gl_ref[cidx] = jnp.broadcast_to(gl, (1, LANES))

        if exchange is not None:
            @pl.when((pl.program_id(0) == NCB // 2) & (h == 0))
            def _():
                exchange.middle(*exchange.split(ex_refs))

            @pl.when((pl.program_id(0) == NCB - 1) & (h == NGH - 1))
            def _():
                exchange.rest(*exchange.split(ex_refs))

    qs, ks, vs, gs, a_s, gl_s = _prep_specs()
    tok = jax.ShapeDtypeStruct((S, DGDN), F32)
    sq = jax.ShapeDtypeStruct((NGH, S, CHUNK), F32)
    res = pl.pallas_call(
        body, name="gdn_prep", grid=(NCB, NGH), in_specs=[qs, ks, vs, gs] + ex_in_specs,
        out_specs=[qs, qs, qs, qs, a_s, gl_s, a_s] + ex_out_specs,
        out_shape=[tok, tok, tok, tok, sq, jax.ShapeDtypeStruct((NGH, NCH, 1, LANES), F32), sq] + ex_out_shape,
        scratch_shapes=ex_scratch, compiler_params=_cparams(),
    )(qkv, qkv, qkv, gates, *ex_in)
    return res[:7], res[7:]


def _gdn_prep_bwd(qkv, gates, t_inv, du, dw, dqd, dkd, da, dgl, exchange=None):
    ex_in, ex_in_specs, ex_out_specs, ex_out_shape, ex_scratch = _hosted(exchange)

    def body(*refs):
        q_ref, k_ref, v_ref, g_ref, t_ref, du_ref, dw_ref, dqd_ref, dkd_ref, da_ref, dgl_ref = refs[:11]
        dqkv_ref, dg_ref = refs[11 + len(ex_in):13 + len(ex_in)]
        ex_refs = refs[11:11 + len(ex_in)] + refs[13 + len(ex_in):]
        h = pl.program_id(1)

        if exchange is not None:
            @pl.when((pl.program_id(0) == 0) & (h == 0))
            def _():
                exchange.start(*exchange.split(ex_refs))

        @pl.when(h == 0)
        def _():
            dg_ref[...] = jnp.zeros_like(dg_ref)

        lane = lax.broadcasted_iota(jnp.int32, (CHUNK, LANES), 1)
        chunks = [pl.ds(cidx * CHUNK, CHUNK) for cidx in range(CB)]
        gts = [g_ref[rows, :] for rows in chunks]
        t_saved = [t_ref[rows, :] for rows in chunks]
        _, vjp = jax.vjp(lambda *args: [o[:6] for o in _chunk_prep(*args, t_saved=t_saved)],
                         [q_ref[rows, :] for rows in chunks], [k_ref[rows, :] for rows in chunks],
                         [v_ref[rows, :] for rows in chunks], [_lane_col(gt, LANE_G + h) for gt in gts],
                         [_lane_col(gt, LANE_BETA + h) for gt in gts])
        dqs, dks, dvs, dgcs, dbcs = vjp([(du_ref[rows, :], dw_ref[rows, :], da_ref[rows, :], dqd_ref[rows, :],
                                          dkd_ref[rows, :], dgl_ref[cidx][:, 0:1]) for cidx, rows in enumerate(chunks)])
        for cidx, rows in enumerate(chunks):
            dq, dk, dv, dgc, dbc = dqs[cidx], dks[cidx], dvs[cidx], dgcs[cidx], dbcs[cidx]
            dqkv_ref[0, rows, :] = dq
            dqkv_ref[1, rows, :] = dk
            dqkv_ref[2, rows, :] = dv
            dg_ref[rows, :] += jnp.where(lane == LANE_G + h, dgc, 0.0) + jnp.where(lane == LANE_BETA + h, dbc, 0.0)

        if exchange is not None:
            @pl.when((pl.program_id(0) == NCB - 1) & (h == NGH - 1))
            def _():
                exchange.finish(*exchange.split(ex_refs))

    qs, ks, vs, gs, a_s, gl_s = _prep_specs()
    res = pl.pallas_call(
        body, name="gdn_prep_bwd", grid=(NCB, NGH), in_specs=[qs, ks, vs, gs, a_s, qs, qs, qs, qs, a_s, gl_s] + ex_in_specs,
        out_specs=[pl.BlockSpec((3, CB * CHUNK, GHD), lambda i, h: (0, i, h)), gs] + ex_out_specs,
        out_shape=[jax.ShapeDtypeStruct((3, S, DGDN), F32), jax.ShapeDtypeStruct((S, LANES), F32)] + ex_out_shape,
        scratch_shapes=ex_scratch, compiler_params=_cparams(),
    )(qkv, qkv, qkv, gates, t_inv, du, dw, dqd, dkd, da, dgl, *ex_in)
    return res[0], res[1], res[2:]


def _scan_specs(nh, parts, reverse):
    wide, rows, chunks = nh * GHD, S // parts, NCH // parts

    def part(p):
        return parts - 1 - p if reverse else p

    hs = pl.BlockSpec((rows, wide), lambda g, p: (part(p), g))
    a_s = pl.BlockSpec((nh, rows, CHUNK), lambda g, p: (g, part(p), 0))
    gl_s = pl.BlockSpec((nh, chunks, 1, LANES), lambda g, p: (g, part(p), 0, 0))
    st_s = pl.BlockSpec((nh, chunks, GHD, GHD), lambda g, p: (g, part(p), 0, 0))
    gz_s = pl.BlockSpec((rows, wide), lambda g, p: (part(p), BLK_GZ // nh + g))
    mix_s = pl.BlockSpec((rows, wide), lambda g, p: (part(p), NPAIR // nh + g))
    return hs, a_s, gl_s, st_s, gz_s, mix_s


def _head_cols(hh):
    return slice(hh * GHD, (hh + 1) * GHD)


SCAN_HEADS, SCAN_PARTS = 4, 2
SCAN_HEADS_BWD, SCAN_PARTS_BWD = 4, 4


def _gdn_scan(u, w, qd, kd, a, gl, proj, w_norm, mix):
    heads = range(SCAN_HEADS)

    def body(u_ref, w_ref, qd_ref, kd_ref, a_ref, gl_ref, z_ref, wn_ref, mix_in, mix_ref, o_ref, st_ref, carry_ref):
        del mix_in

        @pl.when(pl.program_id(1) == 0)
        def _():
            carry_ref[...] = jnp.zeros_like(carry_ref)

        def step(ci, states):
            rows = pl.ds(pl.multiple_of(ci * CHUNK, CHUNK), CHUNK)
            for hh in heads:
                st_ref[hh, ci] = states[hh]
            ws = [_dot(w_ref[rows, _head_cols(hh)], states[hh]) for hh in heads]
            qs = [_dot(qd_ref[rows, _head_cols(hh)], states[hh]) for hh in heads]
            vn = [u_ref[rows, _head_cols(hh)] - ws[hh] for hh in heads]
            av = [_dot(a_ref[hh, rows, :], vn[hh]) for hh in heads]
            kv = [_dot(kd_ref[rows, _head_cols(hh)], vn[hh], 0, 0) for hh in heads]
            for hh in heads:
                o_ref[rows, _head_cols(hh)] = qs[hh] + av[hh]
            return tuple(states[hh] * gl_ref[hh, ci] + kv[hh] for hh in heads)

        last = lax.fori_loop(0, NCH // SCAN_PARTS, step, tuple(carry_ref[hh] for hh in heads))
        for hh in heads:
            carry_ref[hh] = last[hh]
            ov = o_ref[:, _head_cols(hh)]
            mix_ref[:, _head_cols(hh)] = (ov * _rms_scale(ov) * wn_ref[...] * _silu(z_ref[:, _head_cols(hh)])).astype(BF16)

    hs, a_s, gl_s, st_s, gz_s, mix_s = _scan_specs(SCAN_HEADS, SCAN_PARTS, False)
    return pl.pallas_call(
        body, name="gdn_scan", grid=(NGH // SCAN_HEADS, SCAN_PARTS),
        in_specs=[hs, hs, hs, hs, a_s, gl_s, gz_s, pl.BlockSpec((1, GHD), lambda g, p: (0, 0)),
                  pl.BlockSpec(memory_space=pl.ANY)],
        out_specs=[mix_s, hs, st_s],
        out_shape=[jax.ShapeDtypeStruct((S, D), BF16), jax.ShapeDtypeStruct((S, DGDN), F32),
                   jax.ShapeDtypeStruct((NGH, NCH, GHD, GHD), F32)],
        input_output_aliases={8: 0}, scratch_shapes=[pltpu.VMEM((SCAN_HEADS, GHD, GHD), F32)], compiler_params=_cparams(),
    )(u, w, qd, kd, a, gl, proj, w_norm, mix)


def _gdn_scan_bwd(dmix, o, proj, w_norm, u, w, qd, kd, a, gl, states, dproj, exchange=None):
    ex_in, ex_in_specs, ex_out_specs, ex_out_shape, ex_scratch = _hosted(exchange)
    groups = NGH // SCAN_HEADS_BWD

    def body(*refs):
        dy_ref, o_ref, z_ref, wn_ref, u_ref, w_ref, qd_ref, kd_ref, a_ref, gl_ref, st_ref = refs[:11]
        dz_ref, du_ref, dw_ref, dqd_ref, dkd_ref, da_ref, dgl_ref, dwn_ref = refs[12 + len(ex_in):20 + len(ex_in)]
        do_ref, carry_ref = refs[20 + len(ex_in) + len(ex_out_shape):22 + len(ex_in) + len(ex_out_shape)]
        ex_refs = refs[12:12 + len(ex_in)] + refs[20 + len(ex_in):20 + len(ex_in) + len(ex_out_shape)] + refs[-2:]
        heads = range(SCAN_HEADS_BWD)
        chunks = NCH // SCAN_PARTS_BWD

        if exchange is not None:
            @pl.when((pl.program_id(0) == 0) & (pl.program_id(1) == 0))
            def _():
                exchange.start(*exchange.split(ex_refs))

        @pl.when((pl.program_id(0) == 0) & (pl.program_id(1) == 0))
        def _():
            dwn_ref[...] = jnp.zeros_like(dwn_ref)

        @pl.when(pl.program_id(1) == 0)
        def _():
            carry_ref[...] = jnp.zeros_like(carry_ref)

        wn = wn_ref[...]
        for hh in heads:
            c = _head_cols(hh)
            ov = o_ref[:, c]
            zv = z_ref[:, c]
            g = dy_ref[:, c]
            sig = _sigmoid(zv)
            dz_ref[:, c] = (g * (ov * _rms_scale(ov) * wn) * sig * (1.0 + zv * (1.0 - sig))).astype(BF16)
            do, dwt = _rms_bwd(ov, wn, g * zv * sig)
            do_ref[:, c] = do
            dwn_ref[...] += jnp.sum(dwt, axis=0, keepdims=True)

        def step(t, dstates):
            ci = chunks - 1 - t
            rows = pl.ds(pl.multiple_of(ci * CHUNK, CHUNK), CHUNK)
            cols = [_head_cols(hh) for hh in heads]
            state = [st_ref[hh, ci] for hh in heads]
            dov = [do_ref[rows, cols[hh]] for hh in heads]
            wv = [w_ref[rows, cols[hh]] for hh in heads]
            ws = [_dot(wv[hh], state[hh]) for hh in heads]
            adov = [_dot(a_ref[hh, rows, :], dov[hh], 0, 0) for hh in heads]
            kds = [_dot(kd_ref[rows, cols[hh]], dstates[hh]) for hh in heads]
            dqd = [_dot(dov[hh], state[hh], 1, 1) for hh in heads]
            qdo = [_dot(qd_ref[rows, cols[hh]], dov[hh], 0, 0) for hh in heads]
            vn = [u_ref[rows, cols[hh]] - ws[hh] for hh in heads]
            dvn = [adov[hh] + kds[hh] for hh in heads]
            da = [_dot(dov[hh], vn[hh], 1, 1) for hh in heads]
            dkd = [_dot(vn[hh], dstates[hh], 1, 1) for hh in heads]
            dwv = [_dot(dvn[hh], state[hh], 1, 1) for hh in heads]
            wdv = [_dot(wv[hh], dvn[hh], 0, 0) for hh in heads]
            for hh in heads:
                da_ref[hh, rows, :] = da[hh]
                dqd_ref[rows, cols[hh]] = dqd[hh]
                dkd_ref[rows, cols[hh]] = dkd[hh]
                dgl = jnp.sum(jnp.sum(dstates[hh] * state[hh], axis=1, keepdims=True), axis=0, keepdims=True)
                dgl_ref[hh, ci] = jnp.broadcast_to(dgl, (1, LANES))
                du_ref[rows, cols[hh]] = dvn[hh]
                dw_ref[rows, cols[hh]] = -dwv[hh]
            return tuple(dstates[hh] * gl_ref[hh, ci] + qdo[hh] - wdv[hh] for hh in heads)

        last = lax.fori_loop(0, chunks, step, tuple(carry_ref[hh] for hh in heads))
        for hh in heads:
            carry_ref[hh] = last[hh]

        if exchange is not None:
            @pl.when((pl.program_id(0) == groups - 1) & (pl.program_id(1) == SCAN_PARTS_BWD - 1))
            def _():
                exchange.finish(*exchange.split(ex_refs))

    hs, a_s, gl_s, st_s, gz_s, mix_s = _scan_specs(SCAN_HEADS_BWD, SCAN_PARTS_BWD, True)
    vec = pl.BlockSpec((1, GHD), lambda g, p: (0, 0))
    tok = jax.ShapeDtypeStruct((S, DGDN), F32)
    res = pl.pallas_call(
        body, name="gdn_scan_bwd", grid=(groups, SCAN_PARTS_BWD),
        in_specs=[mix_s, hs, gz_s, vec, hs, hs, hs, hs, a_s, gl_s, st_s, pl.BlockSpec(memory_space=pl.ANY)] + ex_in_specs,
        out_specs=[gz_s, hs, hs, hs, hs, a_s, gl_s, vec] + ex_out_specs,
        out_shape=[jax.ShapeDtypeStruct((S, DPROJ_PAD), BF16), tok, tok, tok, tok,
                   jax.ShapeDtypeStruct((NGH, S, CHUNK), F32), jax.ShapeDtypeStruct((NGH, NCH, 1, LANES), F32),
                   jax.ShapeDtypeStruct((1, GHD), F32)] + ex_out_shape,
        input_output_aliases={11: 0},
        scratch_shapes=[pltpu.VMEM((S // SCAN_PARTS_BWD, SCAN_HEADS_BWD * GHD), F32),
                        pltpu.VMEM((SCAN_HEADS_BWD, GHD, GHD), F32)] + ex_scratch,
        compiler_params=_cparams(),
    )(dmix, o, proj, w_norm, u, w, qd, kd, a, gl, states, dproj, *ex_in)
    return res[:8], res[8:]


def _place():
    return lax.axis_index("x"), lax.axis_index("y"), lax.axis_index("c")


def _place_scalars():
    x, y, c = _place()
    return jnp.stack([2 * x + y, c]).astype(jnp.int32)


def _other_chips(x, y):
    return [(1 - x, y), (x, 1 - y), (1 - x, 1 - y)]


HBM = pl.BlockSpec(memory_space=pltpu.HBM)
VMEM = pl.BlockSpec(memory_space=pltpu.VMEM)


def _half_rows(ref_or_rows, half):
    rows = ref_or_rows // 2
    return pl.ds(pl.multiple_of(half * rows, rows), rows)


class _Exchange:
    def __init__(self, inputs, out_shape, n_sems, start, finish=None, middle=None, rest=None):
        self.inputs, self.out_shape, self.n_sems, self.start = inputs, out_shape, n_sems, start
        if finish is None:
            def finish(*refs):
                middle(*refs)
                rest(*refs)
        self.finish = finish
        self.middle = middle if middle is not None else (lambda *refs: None)
        self.rest = rest if rest is not None else finish

    def sem_shapes(self):
        return [pltpu.SemaphoreType.DMA((self.n_sems,)), pltpu.SemaphoreType.DMA((self.n_sems,))]

    def split(self, refs):
        n_in, n_out = len(self.inputs), len(self.out_shape)
        return refs[:n_in], refs[n_in:n_in + n_out], refs[n_in + n_out], refs[n_in + n_out + 1]


def _allgather_exchange(shards, whole=(), sent_rows=None):
    n, nw = len(shards), len(whole)
    slots = 10
    sent_rows = [s.shape[0] for s in shards] if sent_rows is None else sent_rows

    def plan(c, src, outs, send_sems, recv_sems):
        x, y, _ = _place()
        via_x, via_y, diagonal = _other_chips(x, y)
        id_x, id_y, id_diagonal = [2 * chip[0] + chip[1] for chip in (via_x, via_y, diagonal)]
        me, sibling = (x, y, c), (x, y, 1 - c)

        def rows_of(a, half, quarter):
            total = src[a].shape[0]
            first = half * (total // 2) + (0 if quarter is None else quarter * (total // 4))
            size = min(total // 2 if quarter is None else total // 4, sent_rows[a] - first)
            assert size > 0 and size % BF16_ROWS == 0, (a, half, quarter, size)
            return pl.ds(first, size)

        def copy(a, k, chip_index, half, quarter, to, from_src=False):
            rows = rows_of(a, half, quarter)
            dst = outs[a].at[chip_index, rows]
            return pltpu.make_async_remote_copy(
                src_ref=src[a].at[rows] if from_src else dst, dst_ref=dst, send_sem=send_sems.at[slots * a + k],
                recv_sem=recv_sems.at[slots * a + k], device_id=to, device_id_type=MESH)

        def whole_copy(b, k, chip_index, to):
            return pltpu.make_async_remote_copy(
                src_ref=src[n + b], dst_ref=outs[n + b].at[chip_index], send_sem=send_sems.at[slots * n + 3 * b + k],
                recv_sem=recv_sems.at[slots * n + 3 * b + k], device_id=to, device_id_type=MESH)

        first, second, stages, last = [], [], [], []
        own = 2 * x + y
        for a in range(n):
            first += [copy(a, 0, own, c, 0, (*via_x, c), True), copy(a, 1, own, c, 1, (*via_y, c), True)]
            second += [copy(a, 8, own, c, 1, (*via_x, c), True), copy(a, 9, own, c, 0, (*via_y, c), True)]
            stages.append([
                (copy(a, 0, id_x, c, 0, me), [copy(a, 2, id_x, c, 0, (*via_y, c))]),
                (copy(a, 1, id_y, c, 1, me), [copy(a, 3, id_y, c, 1, (*via_x, c))]),
                (copy(a, 8, id_x, c, 1, me), [copy(a, 4, id_x, c, None, sibling)]),
                (copy(a, 9, id_y, c, 0, me), [copy(a, 5, id_y, c, None, sibling)]),
                (copy(a, 2, id_diagonal, c, 0, me), [copy(a, 6, id_diagonal, c, 0, sibling)]),
                (copy(a, 3, id_diagonal, c, 1, me), [copy(a, 7, id_diagonal, c, 1, sibling)]),
            ])
            last += [copy(a, 4, id_x, 1 - c, None, me), copy(a, 5, id_y, 1 - c, None, me),
                     copy(a, 6, id_diagonal, 1 - c, 0, me), copy(a, 7, id_diagonal, 1 - c, 1, me)]
        for b in range(nw):
            for k, (chip, index) in enumerate(((via_x, id_x), (via_y, id_y), (diagonal, id_diagonal))):
                second.append(whole_copy(b, k, own, (*chip, c)))
                last.append(whole_copy(b, k, index, me))
        return first + second, stages, last

    def each_core(step):
        def run(*refs):
            for core in (0, 1):
                @pl.when(lax.axis_index("c") == core)
                def _(core=core):
                    step(core, *refs)

        return run

    def start(core, *refs):
        for cp in plan(core, *refs)[0]:
            cp.start()

    def pass_on(stages, which):
        for stage in which:
            for per_shard in stages:
                lands, onward = per_shard[stage]
                lands.wait_recv()
                for cp in onward:
                    cp.start()

    def middle(core, *refs):
        pass_on(plan(core, *refs)[1], (0, 1, 2, 3))

    def rest(core, *refs):
        first, stages, last = plan(core, *refs)
        pass_on(stages, (4, 5))
        for cp in last:
            cp.wait_recv()
        for cp in first + [cp for per_shard in stages for _, onward in per_shard for cp in onward]:
            cp.wait_send()

    out_shape = [jax.ShapeDtypeStruct((NCHIP,) + s.shape, s.dtype) for s in list(shards) + list(whole)]
    return _Exchange(list(shards) + list(whole), out_shape, slots * n + 3 * nw, each_core(start),
                     middle=each_core(middle), rest=each_core(rest))


def _with_own(gathered, own):
    x, y, _ = _place()
    return lax.dynamic_update_index_in_dim(gathered, own, 2 * x + y, axis=0)


def _simple_exchange(inputs, out_shape, copies_of):
    def start(*refs):
        for cp in copies_of(*refs):
            cp.start()

    def finish(*refs):
        for cp in copies_of(*refs):
            cp.wait()

    return _Exchange(list(inputs), out_shape, len(out_shape) * 3, start, finish)


def _pair_exchange(grads):
    def copies_of(src, outs, send_sems, recv_sems):
        x, y, c = _place()
        return [pltpu.make_async_remote_copy(
            src_ref=src[a].at[:, _half_rows(src[a].shape[1], 1 - c)], dst_ref=outs[a], send_sem=send_sems.at[a],
            recv_sem=recv_sems.at[a], device_id=(x, y, 1 - c), device_id_type=MESH) for a in range(len(src))]

    return _simple_exchange(
        grads, [jax.ShapeDtypeStruct((g.shape[0], g.shape[1] // 2, g.shape[2]), g.dtype) for g in grads], copies_of)


def _pair_sum(grads, theirs, name):
    n = len(grads)

    def body(place_ref, *refs):
        for a in range(n):
            refs[2 * n + a][...] = (refs[a][...].astype(F32) + refs[n + a][...].astype(F32)).astype(BF16)

    def specs(arrs):
        return [pl.BlockSpec((None,) + g.shape[1:], lambda j, place: (j, 0, 0)) for g in arrs]

    own_half = [pl.BlockSpec((None, g.shape[1] // 2, g.shape[2]), lambda j, place: (j, place[1], 0)) for g in grads]
    return pl.pallas_call(
        body, name=name, grid_spec=pltpu.PrefetchScalarGridSpec(
            num_scalar_prefetch=1, grid=(NCHIP,), in_specs=own_half + specs(theirs), out_specs=specs(theirs)),
        out_shape=[jax.ShapeDtypeStruct(g.shape, BF16) for g in theirs], compiler_params=_cparams(),
    )(_place_scalars(), *grads, *theirs)


def _chip_exchange(parts, rows_by_core=None):
    if rows_by_core is None:
        rows_by_core = [(p.shape[1],) * 2 for p in parts]

    def each_core(act):
        def run(src, outs, send_sems, recv_sems):
            x, y, c = _place()
            for core in (0, 1):
                @pl.when(c == core)
                def _(core=core):
                    for a in range(len(src)):
                        rows = pl.ds(0, rows_by_core[a][core])
                        for k, chip in enumerate(_other_chips(x, y)):
                            act(pltpu.make_async_remote_copy(
                                src_ref=src[a].at[2 * chip[0] + chip[1], rows], dst_ref=outs[a].at[k, rows],
                                send_sem=send_sems.at[3 * a + k], recv_sem=recv_sems.at[3 * a + k],
                                device_id=(*chip, core), device_id_type=MESH))

        return run

    return _Exchange(list(parts), [jax.ShapeDtypeStruct((NCHIP - 1,) + p.shape[1:], p.dtype) for p in parts],
                     3 * len(parts), each_core(lambda cp: cp.start()), each_core(lambda cp: cp.wait()))


def _chip_sum(parts, received, exchange=None):
    n = len(parts)
    steps = 4
    ex_in, ex_in_specs, ex_out_specs, ex_out_shape, ex_scratch = _hosted(exchange)
    n_ex = len(ex_in)

    def body(place_ref, *refs):
        mine, theirs = refs[2 * n + n_ex:3 * n + n_ex], refs[3 * n + n_ex:4 * n + n_ex]
        ex_refs = refs[2 * n:2 * n + n_ex] + refs[4 * n + n_ex:len(refs) - n - 2]
        tiles, send_sems, recv_sems = refs[len(refs) - n - 2:len(refs) - 2], refs[-2], refs[-1]
        step = pl.program_id(0)
        if exchange is not None:
            @pl.when(step == 0)
            def _():
                exchange.start(*exchange.split(ex_refs))

        x, y, c = _place()

        def share(a, i):
            rows = tiles[a].shape[1]
            return pltpu.make_async_remote_copy(
                src_ref=tiles[a].at[i], dst_ref=theirs[a].at[pl.ds(pl.multiple_of(i * rows, rows), rows)],
                send_sem=send_sems.at[a * steps + i], recv_sem=recv_sems.at[a * steps + i], device_id=(x, y, 1 - c),
                device_id_type=MESH)

        for a in range(n):
            own, r = refs[a], refs[n + a]
            total = ((own[...].astype(F32) + r[0].astype(F32)) + r[1].astype(F32)) + r[2].astype(F32)
            mine[a][...] = total
            tiles[a][step] = total
            share(a, step).start()

        @pl.when(step == steps - 1)
        def _():
            if exchange is not None:
                exchange.finish(*exchange.split(ex_refs))
            for a in range(n):
                for i in range(steps):
                    share(a, i).wait()

    own_specs = [pl.BlockSpec((None, g.shape[1] // steps, g.shape[2]), lambda i, place: (place[0], i, 0)) for g in parts]
    received_specs = [pl.BlockSpec((g.shape[0], g.shape[1] // steps, g.shape[2]), lambda i, place: (0, i, 0))
                      for g in received]
    out_specs = [pl.BlockSpec((g.shape[1] // steps, g.shape[2]), lambda i, place: (i, 0)) for g in parts]
    halves = [jax.ShapeDtypeStruct(g.shape[1:], F32) for g in parts]
    res = pl.pallas_call(
        body, name="grads_chip_sum", grid_spec=pltpu.PrefetchScalarGridSpec(
            num_scalar_prefetch=1, grid=(steps,), in_specs=own_specs + received_specs + ex_in_specs,
            out_specs=out_specs + [HBM] * n + ex_out_specs,
            scratch_shapes=ex_scratch + [pltpu.VMEM((steps, g.shape[1] // steps, g.shape[2]), F32) for g in parts]
            + [pltpu.SemaphoreType.DMA((n * steps,))] * 2),
        out_shape=halves * 2 + ex_out_shape, compiler_params=_cparams(),
    )(_place_scalars(), *parts, *received, *ex_in)
    return res[:n], res[n:2 * n], res[2 * n:]


def _adamw_math(w, g, m, v):
    nm = ADAM_B1 * m + (1.0 - ADAM_B1) * g
    nv = ADAM_B2 * v + (1.0 - ADAM_B2) * jnp.square(g)
    m_hat = nm / (1.0 - ADAM_B1 ** ADAM_STEP)
    v_hat = nv / (1.0 - ADAM_B2 ** ADAM_STEP)
    return -ADAM_LR * (m_hat / (jnp.sqrt(v_hat) + ADAM_EPS) + ADAM_WD * w), nm, nv


def _adamw_big(ws, g_mine, g_theirs, ms, vs):
    n = len(ws)
    steps = 8
    per_half = steps // 2

    def body(place_ref, *refs):
        outs = refs[5 * n:]
        own_half = (pl.program_id(0) // per_half) == place_ref[1]
        for a in range(n):
            g = jnp.where(own_half, refs[n + a][...], refs[2 * n + a][...])
            d, nm, nv = _adamw_math(refs[a][...], g, refs[3 * n + a][...], refs[4 * n + a][...])
            outs[a][...] = g
            outs[n + a][...] = d
            outs[2 * n + a][...] = nm
            outs[3 * n + a][...] = nv

    specs = [pl.BlockSpec((w.shape[0] // steps, w.shape[1]), lambda i, place: (i, 0)) for w in ws]

    def half_specs(halves, of_this_core):
        def tile(i, place):
            first = (place[1] if of_this_core else 1 - place[1]) * per_half
            return jnp.clip(i - first, 0, per_half - 1), 0

        return [pl.BlockSpec((g.shape[0] // per_half, g.shape[1]), tile) for g in halves]

    shapes = [jax.ShapeDtypeStruct(w.shape, F32) for w in ws]
    res = pl.pallas_call(
        body, name="adamw_big", grid_spec=pltpu.PrefetchScalarGridSpec(
            num_scalar_prefetch=1, grid=(steps,),
            in_specs=specs + half_specs(g_mine, True) + half_specs(g_theirs, False) + specs * 2, out_specs=specs * 4),
        out_shape=shapes * 4, compiler_params=_cparams(),
    )(_place_scalars(), *ws, *g_mine, *g_theirs, *ms, *vs)
    return res[:n], res[n:2 * n], res[2 * n:3 * n], res[3 * n:]


def _adamw_in(w, g_mine, g_theirs, m, v):
    half = D // 2

    def body(w_ref, gm_ref, gt_ref, m_ref, v_ref, g_out, d_out, nm_out, nv_out, g_ref):
        south = lax.axis_index("c") == 0
        g_ref[0:half, :] = jnp.where(south, gm_ref[...], gt_ref[...])
        g_ref[half:D, :] = jnp.where(south, gt_ref[...], gm_ref[...])
        g = g_ref[0:CW, :]
        d, nm, nv = _adamw_math(w_ref[...], g, m_ref[...], v_ref[...])
        g_out[...] = g
        d_out[...] = d
        nm_out[...] = nm
        nv_out[...] = nv

    cols = 2 * LANES
    spec = pl.BlockSpec((CW, cols), lambda i: (0, i))
    half_spec = pl.BlockSpec((half, cols), lambda i: (0, i))
    return pl.pallas_call(
        body, name="adamw_in", grid=(D // cols,), in_specs=[spec, half_spec, half_spec, spec, spec], out_specs=[spec] * 4,
        out_shape=[jax.ShapeDtypeStruct((CW, D), F32)] * 4, scratch_shapes=[pltpu.VMEM((D, cols), F32)],
        compiler_params=_cparams(),
    )(w, g_mine, g_theirs, m, v)


NORM_NAMES = ("pre_mix_norm", "post_mix_norm", "pre_mlp_norm", "post_mlp_norm")
SMALL_NAMES = NORM_NAMES + ("gdn_conv_w", "fox_f_bias", "gdn_dt_bias", "gdn_a_log", "fox_out_norm", "gdn_out_norm")
CONV_COLS = 3 * DGDN // NCHIP


def _small_gather(d_norms, d_conv, sums, d_fox_norm, d_gdn_norm, loss_row):
    n_arrays = 6
    n_remote = n_arrays * (NDEV - 1)

    def copies_of(src, outs, send_sems, recv_sems):
        x, y, c = _place()
        me = 4 * x + 2 * y + c

        def from_me(chip_index):
            cols = pl.ds(pl.multiple_of(chip_index * CONV_COLS, LANES), CONV_COLS)
            return [src[0], src[1].at[:, cols], src[2], src[3], src[4], src[5]]

        local = [pltpu.make_async_copy(s, outs[a].at[me], send_sems.at[n_remote + a]) for a, s in enumerate(from_me(2 * x + y))]
        remote = []
        for k in range(1, NDEV):
            px, py, pc = x ^ ((k >> 2) & 1), y ^ ((k >> 1) & 1), c ^ (k & 1)
            remote += [pltpu.make_async_remote_copy(
                src_ref=s, dst_ref=outs[a].at[me], send_sem=send_sems.at[n_arrays * (k - 1) + a],
                recv_sem=recv_sems.at[n_arrays * (k - 1) + a], device_id=(px, py, pc), device_id_type=MESH)
                for a, s in enumerate(from_me(2 * px + py))]
        return local + remote

    def start(*refs):
        for cp in copies_of(*refs):
            cp.start()

    def finish(*refs):
        for cp in copies_of(*refs):
            cp.wait()

    shapes = [(4, D), (CONV_K, CONV_COLS), (8, LANES), (1, LANES), (1, LANES), (1, LANES)]
    return _Exchange([d_norms, d_conv, sums, d_fox_norm, d_gdn_norm, loss_row],
                     [jax.ShapeDtypeStruct((NDEV,) + s, F32) for s in shapes], n_remote + n_arrays, start, finish)


def _small_adamw(gathered, ws, ms, vs):
    n = len(SMALL_NAMES)
    ng = len(gathered)

    def body(*refs):
        def total(buf):
            acc = buf[0]
            for i in range(1, NDEV):
                acc = acc + buf[i]
            return acc

        t_norms, t_conv, t_sums, t_fn, t_gn, t_loss = [total(r) for r in refs[:ng]]
        w_refs, m_refs, v_refs = refs[ng:ng + n], refs[ng + n:ng + 2 * n], refs[ng + 2 * n:ng + 3 * n]
        outs = refs[ng + 3 * n:]
        outs[4 * n][...] = t_loss
        grads = [t_norms[i:i + 1, :] for i in range(4)] + [
            t_conv, t_sums[0:1, 0:NFH], t_sums[1:2, 0:NGH], t_sums[2:3, 0:NGH], t_fn[:, 0:FHD], t_gn]
        for a in range(n):
            d, nm, nv = _adamw_math(w_refs[a][...], grads[a], m_refs[a][...], v_refs[a][...])
            outs[a][...] = grads[a]
            outs[n + a][...] = d
            outs[2 * n + a][...] = nm
            outs[3 * n + a][...] = nv

    def whole(arr):
        return pl.BlockSpec(arr.shape, lambda i: (0,) * arr.ndim)

    res = pl.pallas_call(
        body, name="small_adamw", grid=(1,), in_specs=[whole(t) for t in gathered] + [whole(w) for w in ws] * 3,
        out_specs=[whole(w) for w in ws] * 4 + [pl.BlockSpec((1, LANES), lambda i: (0, 0))],
        out_shape=[jax.ShapeDtypeStruct(w.shape, F32) for w in ws] * 4 + [jax.ShapeDtypeStruct((1, LANES), F32)],
        compiler_params=_cparams(),
    )(*gathered, *ws, *ms, *vs)
    return res[:n], res[n:2 * n], res[2 * n:3 * n], res[3 * n:4 * n], res[4 * n]


CW = DPROJ // NCHIP
BF16_ROWS = 16
CW_SENT = -(-CW // BF16_ROWS) * BF16_ROWS
PROJ_RUNS = tuple((part * DFOX + hp * LANES, part * DFOX + (hp + 1) * LANES, (3 * hp + part) * LANES)
                  for hp in range(NPAIR) for part in range(3)) + (
    (1536, 1544, BLK_SMALL * LANES), (1544, 3080, BLK_GDN * LANES), (3080, 3088, BLK_SMALL * LANES + 8),
    (3088, 3600, BLK_GZ * LANES))


def _proj_pieces():
    pieces = []
    for lo, hi, at in PROJ_RUNS:
        while lo < hi:
            j = lo // CW
            end = min(hi, (j + 1) * CW)
            pieces.append((j, lo - j * CW, at, end - lo))
            at, lo = at + end - lo, end
    return pieces


RT = 256


def _to_padded_rows(gathered):
    def body(src_ref, out_ref, blocks_ref, rows_ref):
        blocks_ref[...] = src_ref[...].astype(F32)
        rows_ref[...] = jnp.zeros_like(rows_ref)
        for j, start, at, n in _proj_pieces():
            rows_ref[at:at + n, :] = blocks_ref[j, start:start + n, :]
        out_ref[...] = rows_ref[...].astype(out_ref.dtype)

    return pl.pallas_call(
        body, name="proj_rows_in", grid=(D // RT,), in_specs=[pl.BlockSpec((NCHIP, D, RT), lambda i: (0, 0, i))],
        out_specs=pl.BlockSpec((DPROJ_PAD, RT), lambda i: (0, i)), out_shape=jax.ShapeDtypeStruct((DPROJ_PAD, D), gathered.dtype),
        scratch_shapes=[pltpu.VMEM((NCHIP, D, RT), F32), pltpu.VMEM((DPROJ_PAD, RT), F32)], compiler_params=_cparams(),
    )(gathered)


def _from_padded_rows_pair_sum(w):
    steps = D // RT
    half = D // 2

    def body(src_ref, out_ref, rows_ref, blocks_ref, mine_ref, send_ref, recv_ref, send_sems, recv_sems):
        i = pl.program_id(0)
        x, y, c = _place()

        def share(t):
            return pltpu.make_async_remote_copy(
                src_ref=send_ref.at[t], dst_ref=recv_ref.at[t], send_sem=send_sems.at[t], recv_sem=recv_sems.at[t],
                device_id=(x, y, 1 - c), device_id_type=MESH)

        def rows_of(core):
            return blocks_ref[:, pl.ds(pl.multiple_of(core * half, half), half), :]

        @pl.when(i < steps)
        def _():
            rows_ref[...] = src_ref[...].astype(F32)
            blocks_ref[...] = jnp.zeros_like(blocks_ref)
            for j, start, at, n in _proj_pieces():
                blocks_ref[j, start:start + n, :] = rows_ref[at:at + n, :]
            mine_ref[i % 2] = rows_of(c)
            send_ref[i] = rows_of(1 - c).astype(BF16)
            share(i).start()

        @pl.when(i > 0)
        def _():
            share(i - 1).wait_recv()
            out_ref[...] = (mine_ref[(i - 1) % 2] + recv_ref[i - 1].astype(F32)).astype(BF16)

        @pl.when(i == steps)
        def _():
            for t in range(steps):
                share(t).wait_send()

    tile = (NCHIP, half, RT)
    return pl.pallas_call(
        body, name="proj_rows_out_pair_sum", grid=(steps + 1,),
        in_specs=[pl.BlockSpec((DPROJ_PAD, RT), lambda i: (0, jnp.minimum(i, steps - 1)))],
        out_specs=pl.BlockSpec(tile, lambda i: (0, 0, jnp.maximum(i - 1, 0))),
        out_shape=jax.ShapeDtypeStruct((NCHIP, half, D), BF16),
        scratch_shapes=[pltpu.VMEM((DPROJ_PAD, RT), F32), pltpu.VMEM((NCHIP, D, RT), F32), pltpu.VMEM((2,) + tile, F32),
                        pltpu.VMEM((steps,) + tile, BF16), pltpu.VMEM((steps,) + tile, BF16),
                        pltpu.SemaphoreType.DMA((steps,)), pltpu.SemaphoreType.DMA((steps,))],
        compiler_params=_cparams(),
    )(w)


def _local_step(x, target, first_weights, late_weights, reduce_late, reduce_in, pre_mix_norm, fox_f_bias, fox_out_norm,
                gdn_a_log, gdn_dt_bias, gdn_out_norm, post_mix_norm, pre_mlp_norm, post_mlp_norm):
    bias_vec = jnp.zeros((1, LANES), F32).at[0, 0:NFH].set(fox_f_bias).at[0, LANE_G:LANE_G + NGH].set(gdn_dt_bias)
    alog_vec = jnp.zeros((1, LANES), F32).at[0, LANE_G:LANE_G + NGH].set(gdn_a_log)
    w2 = jnp.concatenate([fox_out_norm, fox_out_norm], axis=1)

    h, first = _pre_norm(x, pre_mix_norm, exchange=first_weights[0])
    win_p, conv_w = first_weights[1](first)
    proj = _matmul(h, win_p, tb=True, tm=2048, tn=768, tk=1024, name="mm_proj", exchange=late_weights[0])
    proj, late_a = proj if late_weights[0] is not None else (proj, [])
    gates = _gates(proj, bias_vec, alog_vec)
    mix, fox_o, lse, late_b = _fox_fwd(proj, gates, w2, exchange=late_weights[1])
    qkv = _gdn_pre(proj, conv_w)
    (u, w, qd, kd, a_intra, gl, t_inv), _ = _gdn_prep(qkv, gates)
    wout, wup3 = late_weights[3](late_a, late_b)
    mix, gdn_raw, states = _gdn_scan(u, w, qd, kd, a_intra, gl, proj, gdn_out_norm, mix)
    def post_mix(acc, xv, w_post, w_pre_mlp):
        x1v = xv + acc * _rms_scale(acc) * w_post
        return acc, x1v, x1v * _rms_scale(x1v) * w_pre_mlp

    mixed, x1, h2 = _matmul(mix, wout, tm=512, tn=D, tk=1024, out_dtypes=(F32, F32, BF16), name="mm_out",
                            extra=(x, post_mix_norm, pre_mlp_norm), epilogue=post_mix)

    def relu2(acc):
        r = jnp.maximum(acc, 0.0)
        return r, r * r

    up_act = _matmul(h2, wup3, b3=True, tm=1024, tn=1024, tk=1024, out_dtypes=(BF16, BF16), epilogue=relu2,
                     name="mm_up", exchange=late_weights[2])
    (up_relu, act), late_c = up_act if late_weights[2] is not None else (up_act, [])
    wdown = late_weights[4](late_c)
    def loss_head(acc, x1v, tv, w):
        err = x1v + acc * _rms_scale(acc) * w - tv
        dx2v = err * (1.0 / D)
        dyv, dwt = _rms_bwd(acc, w, dx2v)
        part = 0.5 * jnp.sum(jnp.mean(err * err, axis=-1, keepdims=True), axis=0, keepdims=True)
        return dx2v, dyv, jnp.sum(dwt, axis=0, keepdims=True), jnp.broadcast_to(part, (1, D))

    dx2, dy, d_post_mlp, loss_wide = _matmul(
        act, wdown, tm=512, tn=D, tk=DFF, out_dtypes=(F32, BF16, F32, F32), extra=(x1, target, post_mlp_norm),
        epilogue=loss_head, n_sums=2, name="mm_down")
    loss_row = loss_wide[:, :LANES]

    dwdown = _matmul(act, dy, ta=True, tm=1024, tn=1024, tk=2048, out_dtypes=(BF16,), name="mm_dwdown")

    def relu2_bwd(acc, r):
        return (acc * 2.0 * r.astype(F32),)

    dup = _matmul(dy, wdown, tb=True, tm=1024, tn=1024, tk=1024, out_dtypes=(BF16,), extra=(up_relu,), epilogue=relu2_bwd,
                  name="mm_dact")
    dwup3 = _matmul(h2, dup, ta=True, tm=1024, tn=1024, tk=2048, out_dtypes=(BF16,), o3=True, name="mm_dwup")
    def mid_bwd(acc, x1v, dx2v, mixedv, w_pre_mlp, w_post):
        dxa, dwm = _rms_bwd(x1v, w_pre_mlp, acc)
        dx1v = dx2v + dxa
        dm, dwp = _rms_bwd(mixedv, w_post, dx1v)
        return dx1v, dm, jnp.sum(dwm, axis=0, keepdims=True), jnp.sum(dwp, axis=0, keepdims=True)

    dx1, dmixed, d_pre_mlp, d_post_mix = _matmul(
        dup, wup3, tb=True, b3=True, tm=512, tn=D, tk=DFF, out_dtypes=(F32, BF16, F32, F32),
        extra=(x1, dx2, mixed, pre_mlp_norm, post_mix_norm), epilogue=mid_bwd, n_sums=2, name="mm_dh2")
    dwout = _matmul(mix, dmixed, ta=True, tm=256, tn=1024, tk=2048, out_dtypes=(BF16,), name="mm_dwout")
    dmix = _matmul(dmixed, wout, tb=True, tm=512, tn=1024, tk=1024, name="mm_dmix")

    dfox, delta, d_fox_norm, from_sibling = _fox_norm_bwd(fox_o, dmix, w2, exchange=reduce_late[0](dwout, dwup3, dwdown))
    dproj, dcum_fox, reduced_a = _fox_bwd(proj, dfox, gates, lse, delta, exchange=reduce_late[1](from_sibling))
    (dproj, du, dw, dqd, dkd, da, dgl, d_gdn_norm), reduced_b = _gdn_scan_bwd(
        dmix, gdn_raw, proj, gdn_out_norm, u, w, qd, kd, a_intra, gl, states, dproj, exchange=reduce_late[2]())
    dqkv, dgates_gdn, reduced_c = _gdn_prep_bwd(qkv, gates, t_inv, du, dw, dqd, dkd, da, dgl, exchange=reduce_late[3]())
    reduced_late = (reduced_a, reduced_b, reduced_c)
    dproj, d_conv = _gdn_pre_bwd(proj, conv_w, dqkv, dproj)
    dproj, sums = _gates_bwd(proj, bias_vec, alog_vec, dgates_gdn, dcum_fox, dproj)

    dwin_p = _matmul(dproj, h, ta=True, tm=1280, tn=1024, tk=2048, out_dtypes=(BF16,), name="mm_dwin")
    exchange_in = reduce_in(dwin_p)
    dh = _matmul(dproj, win_p, tm=1024, tk=DPROJ_PAD, name="mm_dh", exchange=exchange_in)
    dh, reduced_in = dh if exchange_in is not None else (dh, [])
    grad_x, d_pre_mix = _pre_norm_bwd(dh, x, pre_mix_norm, dx1)

    d_norms = jnp.concatenate([d_pre_mix, d_post_mix, d_pre_mlp, d_post_mlp], axis=0)
    return grad_x, (d_norms, d_conv, sums, d_fox_norm, d_gdn_norm, loss_row), reduced_late, reduced_in


def kernel(x, pre_mix_norm, w_in, fox_f_bias, fox_out_norm, gdn_conv_w, gdn_a_log, gdn_dt_bias, gdn_out_norm, w_out, post_mix_norm, pre_mlp_norm, w_up, w_down, post_mlp_norm, loss_target, m_pre_mix_norm, m_w_in, m_fox_f_bias, m_fox_out_norm, m_gdn_conv_w, m_gdn_a_log, m_gdn_dt_bias, m_gdn_out_norm, m_w_out, m_post_mix_norm, m_pre_mlp_norm, m_w_up, m_w_down, m_post_mlp_norm, v_pre_mix_norm, v_w_in, v_fox_f_bias, v_fox_out_norm, v_gdn_conv_w, v_gdn_a_log, v_gdn_dt_bias, v_gdn_out_norm, v_w_out, v_post_mix_norm, v_pre_mlp_norm, v_w_up, v_w_down, v_post_mlp_norm):
    weights = dict(pre_mix_norm=pre_mix_norm, w_in=w_in, fox_f_bias=fox_f_bias, fox_out_norm=fox_out_norm, gdn_conv_w=gdn_conv_w,
                   gdn_a_log=gdn_a_log, gdn_dt_bias=gdn_dt_bias, gdn_out_norm=gdn_out_norm, w_out=w_out, post_mix_norm=post_mix_norm,
                   pre_mlp_norm=pre_mlp_norm, w_up=w_up, w_down=w_down, post_mlp_norm=post_mlp_norm)
    m_in = dict(pre_mix_norm=m_pre_mix_norm, w_in=m_w_in, fox_f_bias=m_fox_f_bias, fox_out_norm=m_fox_out_norm, gdn_conv_w=m_gdn_conv_w,
                gdn_a_log=m_gdn_a_log, gdn_dt_bias=m_gdn_dt_bias, gdn_out_norm=m_gdn_out_norm, w_out=m_w_out, post_mix_norm=m_post_mix_norm,
                pre_mlp_norm=m_pre_mlp_norm, w_up=m_w_up, w_down=m_w_down, post_mlp_norm=m_post_mlp_norm)
    v_in = dict(pre_mix_norm=v_pre_mix_norm, w_in=v_w_in, fox_f_bias=v_fox_f_bias, fox_out_norm=v_fox_out_norm, gdn_conv_w=v_gdn_conv_w,
                gdn_a_log=v_gdn_a_log, gdn_dt_bias=v_gdn_dt_bias, gdn_out_norm=v_gdn_out_norm, w_out=v_w_out, post_mix_norm=v_post_mix_norm,
                pre_mlp_norm=v_pre_mlp_norm, w_up=v_w_up, w_down=v_w_down, post_mlp_norm=v_post_mlp_norm)
    order_w = ("pre_mix_norm", "w_in", "fox_f_bias", "fox_out_norm", "gdn_conv_w", "gdn_a_log", "gdn_dt_bias", "gdn_out_norm", "w_out",
               "post_mix_norm", "pre_mlp_norm", "w_up", "w_down", "post_mlp_norm")
    big = ("w_in", "w_out", "w_up", "w_down")

    def row(v):
        return v if v.ndim == 2 else v.reshape(1, -1)

    win_shard = jnp.pad(w_in.T.astype(BF16), ((0, D - CW), (0, 0)))

    def resolve_first(gathered):
        win_g, conv_g = gathered
        return (_to_padded_rows(_with_own(win_g, win_shard)),
                _with_own(conv_g, gdn_conv_w).transpose(1, 0, 2).reshape(CONV_K, 3 * DGDN))

    late_shards = [weights[n].astype(BF16) for n in big[1:]]

    gathered_down = []

    def resolve_out_up(gathered_out, gathered_mlp):
        gathered_down.append(gathered_mlp[1])
        return _with_own(gathered_out[0], late_shards[0]).reshape(D, D), _with_own(gathered_mlp[0], late_shards[1])

    def resolve_down(_):
        return _with_own(gathered_down[0], late_shards[2]).reshape(DFF, D)

    pair_sums, late_blocks = {}, []

    def pair_summed(names, blocks, theirs):
        for n, s in zip(names, _pair_sum(blocks, theirs, "grads_pair_sum_" + names[0])):
            pair_sums[n] = s

    def late_pair_exchange(dwout, dwup3, dwdown):
        late_blocks.extend([dwout.reshape(NCHIP, D // NCHIP, D), dwup3, dwdown.reshape(NCHIP, DFF // NCHIP, D)])
        return _pair_exchange(late_blocks)

    def late_chip_exchange(theirs):
        pair_summed(big[1:], late_blocks, theirs)
        return _chip_exchange([pair_sums["w_up"], pair_sums["w_down"]])

    def reduce_in(dwin_p):
        pair_sums["w_in"] = _from_padded_rows_pair_sum(dwin_p)
        return _chip_exchange([pair_sums["w_in"]], rows_by_core=[(D // 2, CW_SENT - D // 2)])

    grad_x, small, received_late, received_in = _local_step(
        x[0], loss_target[0], (_allgather_exchange([win_shard], whole=[gdn_conv_w], sent_rows=[CW_SENT]), resolve_first),
        (_allgather_exchange(late_shards[:1]), _allgather_exchange(late_shards[1:]), None, resolve_out_up, resolve_down),
        (late_pair_exchange, late_chip_exchange, lambda: None, lambda: _chip_exchange([pair_sums["w_out"]])),
        reduce_in, row(pre_mix_norm), fox_f_bias, row(fox_out_norm), gdn_a_log, gdn_dt_bias,
        row(gdn_out_norm), row(post_mix_norm), row(pre_mlp_norm), row(post_mlp_norm))
    received_mlp, _, received_out = received_late

    g_mine, g_theirs, small_gathered = _chip_sum(
        [pair_sums[n] for n in big], list(received_in[:1]) + list(received_out[:1]) + list(received_mlp[:2]),
        exchange=_small_gather(*small))

    g_big, d_big, nm_big, nv_big = _adamw_big(
        [weights[n] for n in big[1:]], g_mine[1:], g_theirs[1:], [m_in[n] for n in big[1:]], [v_in[n] for n in big[1:]])
    in_t = _adamw_in(w_in.T, g_mine[0], g_theirs[0], m_w_in.T, v_w_in.T)
    g_small, d_small, nm_small, nv_small, loss_total = _small_adamw(
        small_gathered, [row(weights[n]) for n in SMALL_NAMES], [row(m_in[n]) for n in SMALL_NAMES],
        [row(v_in[n]) for n in SMALL_NAMES])

    grads, delta, new_m, new_v = {}, {}, {}, {}
    grads["w_in"], delta["w_in"], new_m["w_in"], new_v["w_in"] = [t.T for t in in_t]
    for i, n in enumerate(big[1:]):
        grads[n], delta[n], new_m[n], new_v[n] = g_big[i], d_big[i], nm_big[i], nv_big[i]
    for i, n in enumerate(SMALL_NAMES):
        shape = weights[n].shape
        grads[n], delta[n], new_m[n], new_v[n] = (g_small[i].reshape(shape), d_small[i].reshape(shape),
                                                  nm_small[i].reshape(shape), nv_small[i].reshape(shape))
    return (loss_total[0, 0], grad_x[None], *[grads[n] for n in order_w], *[delta[n] for n in order_w], *[new_m[n] for n in order_w],
            *[new_v[n] for n in order_w])
```

```python
import jax
import jax.numpy as jnp
from jax import lax
from jax.experimental import pallas as pl
from jax.experimental.pallas import tpu as pltpu

F32 = jnp.float32
BF16 = jnp.bfloat16
MESH = pl.DeviceIdType.MESH

S = 2048
D = 1024
NFH, FHD = 8, 64
NPAIR = NFH // 2
NGH, GHD = 4, 128
DFOX = NFH * FHD
DGDN = NGH * GHD
CHUNK = 64
NCH = S // CHUNK
CONV_K = 4
DFF = 4 * D
EPS = 1e-6
DPROJ = 3600
LANES = 128
DPROJ_PAD = 3840
BLK_GDN = 12
BLK_GZ = 24
BLK_SMALL = 28
NCHIP = 4
NDEV = 8
VMEM_LIMIT = 56 * 1024 * 1024

ADAM_LR = 0.001
ADAM_B1 = 0.9
ADAM_B2 = 0.999
ADAM_EPS = 1e-08
ADAM_WD = 0.01
ADAM_STEP = 10


def _cparams(**kw):
    return pltpu.CompilerParams(vmem_limit_bytes=VMEM_LIMIT, **kw)


def _dn(ca, cb):
    return (((ca,), (cb,)), ((), ()))


def _dot(a, b, ca=1, cb=0):
    return lax.dot_general(a.astype(BF16), b.astype(BF16), _dn(ca, cb), preferred_element_type=F32)


def _hdot(a, b, ca=1, cb=0):
    return lax.dot_general(a.astype(F32), b.astype(F32), _dn(ca, cb), precision=lax.Precision.HIGHEST,
                           preferred_element_type=F32)


def _dot3(a, b, ca=1, cb=0):
    a_hi, b_hi = a.astype(BF16), b.astype(BF16)
    a_lo, b_lo = (a - a_hi.astype(F32)).astype(BF16), (b - b_hi.astype(F32)).astype(BF16)
    dn = _dn(ca, cb)
    return (lax.dot_general(a_hi, b_hi, dn, preferred_element_type=F32)
            + (lax.dot_general(a_hi, b_lo, dn, preferred_element_type=F32)
               + lax.dot_general(a_lo, b_hi, dn, preferred_element_type=F32)))


@jax.custom_vjp
def _mm_nn(a, b):
    return _dot(a, b, 1, 0)


def _mm_nn_fwd(a, b):
    return _dot(a, b, 1, 0), (a, b)


def _mm_nn_bwd(res, g):
    a, b = res
    return _dot(g, b, 1, 1), _dot(a, g, 0, 0)


_mm_nn.defvjp(_mm_nn_fwd, _mm_nn_bwd)


@jax.custom_vjp
def _mm_nt(a, b):
    return _dot(a, b, 1, 1)


def _mm_nt_fwd(a, b):
    return _dot(a, b, 1, 1), (a, b)


def _mm_nt_bwd(res, g):
    a, b = res
    return _dot(g, b, 1, 0), _dot(g, a, 0, 0)


_mm_nt.defvjp(_mm_nt_fwd, _mm_nt_bwd)


@jax.custom_vjp
def _saved_inverse(m, t_inv):
    del m
    return t_inv


def _saved_inverse_fwd(m, t_inv):
    del m
    return t_inv, t_inv


def _saved_inverse_bwd(t_inv, g):
    return -_dot3(_dot3(t_inv, g, 0, 0), t_inv, 1, 1), jnp.zeros_like(t_inv)


_saved_inverse.defvjp(_saved_inverse_fwd, _saved_inverse_bwd)


def _sigmoid(z):
    return 1.0 / (1.0 + jnp.exp(-z))


def _softplus(z):
    return jnp.maximum(z, 0.0) + jnp.log(1.0 + jnp.exp(-jnp.abs(z)))


def _silu(z):
    return z * _sigmoid(z)


def _rms_scale(x):
    return lax.rsqrt(jnp.mean(x * x, axis=-1, keepdims=True) + EPS)


def _rms_bwd(x, w, g):
    r = _rms_scale(x)
    gw = g * w
    dx = r * gw - x * (r * r * r) * jnp.mean(gw * x, axis=-1, keepdims=True)
    return dx, g * x * r


def _matmul(a, b, *, name, ta=False, tb=False, tm=512, tn=512, tk=512, out_dtypes=(F32,), b3=False, o3=False,
            extra=(), epilogue=None, exchange=None, n_sums=0):
    m, k = (a.shape[1], a.shape[0]) if ta else a.shape
    if b3:
        n = b.shape[1] if tb else b.shape[0] * b.shape[2]
        kb = b.shape[0] * b.shape[2] if tb else b.shape[1]
    else:
        n, kb = (b.shape[0], b.shape[1]) if tb else (b.shape[1], b.shape[0])
    assert kb == k, (name, kb, k)
    tm, tn, tk = min(tm, m), min(tn, n), min(tk, k)
    assert m % tm == 0 and n % tn == 0 and k % tk == 0, (name, m, n, k, tm, tn, tk)
    nk = k // tk
    whole_k_blocks = b3 and tb and not ta and nk == 1 and b.shape[0] > 1
    n_extra = len(extra)
    n_out = len(out_dtypes)
    grid = (m // tm, n // tn, nk)
    ex_in, ex_in_specs, ex_out_specs, ex_out_shape, ex_scratch = _hosted(exchange)

    def body(*refs):
        a_ref, b_ref = refs[0], refs[1]
        extra_refs = refs[2:2 + n_extra]
        first_out = 2 + n_extra + len(ex_in)
        out_refs = refs[first_out:first_out + n_out]
        ex_refs = refs[2 + n_extra:first_out] + refs[first_out + n_out:first_out + n_out + len(ex_out_shape)] + refs[-2:]
        step = [pl.program_id(d) for d in range(3)]

        if exchange is not None:
            @pl.when((step[0] == 0) & (step[1] == 0) & (step[2] == 0))
            def _():
                exchange.start(*exchange.split(ex_refs))

        def finish(acc):
            outs = (acc,) if epilogue is None else epilogue(acc, *[r[...] for r in extra_refs])
            for o_ref, val in zip(out_refs[:n_out - n_sums], outs):
                o_ref[...] = val.astype(o_ref.dtype)
            for o_ref, val in zip(out_refs[n_out - n_sums:], outs[n_out - n_sums:]):
                @pl.when(step[0] == 0)
                def _(o_ref=o_ref, val=val):
                    o_ref[...] = val

                @pl.when(step[0] > 0)
                def _(o_ref=o_ref, val=val):
                    o_ref[...] += val

        if whole_k_blocks:
            width = b.shape[2]
            part = _dot(a_ref[:, 0:width], b_ref[0], 1, 1)
            for blk in range(1, b.shape[0]):
                part = part + _dot(a_ref[:, blk * width:(blk + 1) * width], b_ref[blk], 1, 1)
        else:
            part = _dot(a_ref[...], b_ref[...], 0 if ta else 1, 1 if tb else 0)
        if nk == 1:
            finish(part)
        else:
            acc_ref = refs[first_out + n_out + len(ex_out_shape)]

            @pl.when(step[2] == 0)
            def _():
                acc_ref[...] = part

            @pl.when(step[2] > 0)
            def _():
                acc_ref[...] += part

            @pl.when(step[2] == nk - 1)
            def _():
                finish(acc_ref[...])

        if exchange is not None:
            flat = (step[0] * grid[1] + step[1]) * nk + step[2]
            total = grid[0] * grid[1] * nk

            @pl.when(flat == total // 2)
            def _():
                exchange.middle(*exchange.split(ex_refs))

            @pl.when(flat == total - 1)
            def _():
                exchange.rest(*exchange.split(ex_refs))

    a_spec = pl.BlockSpec((tk, tm), lambda i, j, kk: (kk, i)) if ta else pl.BlockSpec((tm, tk), lambda i, j, kk: (i, kk))
    if whole_k_blocks:
        b_spec = pl.BlockSpec((b.shape[0], tn, b.shape[2]), lambda i, j, kk: (0, j, 0))
    elif b3 and tb:
        assert b.shape[2] == tk
        b_spec = pl.BlockSpec((None, tn, tk), lambda i, j, kk: (kk, j, 0))
    elif b3:
        assert b.shape[2] == tn
        b_spec = pl.BlockSpec((None, tk, tn), lambda i, j, kk: (j, kk, 0))
    elif tb:
        b_spec = pl.BlockSpec((tn, tk), lambda i, j, kk: (j, kk))
    else:
        b_spec = pl.BlockSpec((tk, tn), lambda i, j, kk: (kk, j))
    tile = pl.BlockSpec((tm, tn), lambda i, j, kk: (i, j))
    out_specs = [tile] * n_out
    out_shape = [jax.ShapeDtypeStruct((m, n), dt) for dt in out_dtypes]
    if o3:
        out_specs[0] = pl.BlockSpec((None, tm, tn), lambda i, j, kk: (j, i, 0))
        out_shape[0] = jax.ShapeDtypeStruct((n // tn, m, tn), out_dtypes[0])
    assert n_sums == 0 or tn == n
    for r in range(n_out - n_sums, n_out):
        out_specs[r] = pl.BlockSpec((1, tn), lambda i, j, kk: (0, 0))
        out_shape[r] = jax.ShapeDtypeStruct((1, n), out_dtypes[r])
    res = pl.pallas_call(
        body, name=name, grid=grid,
        in_specs=[a_spec, b_spec] + [tile if e.shape[0] == m else pl.BlockSpec((1, tn), lambda i, j, kk: (0, j)) for e in extra]
        + ex_in_specs, out_specs=out_specs + ex_out_specs,
        out_shape=out_shape + ex_out_shape,
        scratch_shapes=([pltpu.VMEM((tm, tn), F32)] if nk > 1 else []) + ex_scratch,
        compiler_params=_cparams(),
    )(a, b, *extra, *ex_in)
    if exchange is not None:
        return (res[0] if n_out == 1 else res[:n_out]), res[n_out:]
    return res[0] if n_out == 1 else res


TR = 512


def _row_spec(cols):
    return pl.BlockSpec((TR, cols), lambda i: (i, 0))


def _vec_spec(cols):
    return pl.BlockSpec((1, cols), lambda i: (0, 0))


def _pre_norm(x, w, exchange=None):
    ex_in, ex_in_specs, ex_out_specs, ex_out_shape, ex_scratch = _hosted(exchange)

    def body(*refs):
        x_ref, w_ref, h_ref = refs[0], refs[1], refs[2 + len(ex_in)]
        ex_refs = refs[2:2 + len(ex_in)] + refs[3 + len(ex_in):]
        if exchange is not None:
            @pl.when(pl.program_id(0) == 0)
            def _():
                exchange.start(*exchange.split(ex_refs))

        xv = x_ref[...]
        h_ref[...] = (xv * _rms_scale(xv) * w_ref[...]).astype(BF16)

        if exchange is not None:
            @pl.when(pl.program_id(0) == S // TR - 1)
            def _():
                exchange.finish(*exchange.split(ex_refs))

    res = pl.pallas_call(
        body, name="pre_norm", grid=(S // TR,), in_specs=[_row_spec(D), _vec_spec(D)] + ex_in_specs,
        out_specs=[_row_spec(D)] + ex_out_specs, out_shape=[jax.ShapeDtypeStruct((S, D), BF16)] + ex_out_shape,
        scratch_shapes=ex_scratch, compiler_params=_cparams(),
    )(x, w, *ex_in)
    return res[0], res[1:]


def _mm_out(mix, wout, x, epilogue, *rows):
    steps, slots, ahead = S // TR, 3, 2

    def body(mix_hbm, wout_ref, x_hbm, *refs):
        row_refs, outs = refs[:len(rows)], refs[len(rows):len(rows) + 3]
        mix_buf, x_buf, sems = refs[len(rows) + 3:]
        i = pl.program_id(0)

        def fetch(t):
            tile = pl.ds(pl.multiple_of(t * TR, TR), TR)
            return (pltpu.make_async_copy(mix_hbm.at[tile], mix_buf.at[t % slots], sems.at[0, t % slots]),
                    pltpu.make_async_copy(x_hbm.at[tile], x_buf.at[t % slots], sems.at[1, t % slots]))

        @pl.when(i == 0)
        def _():
            for t in range(ahead):
                for cp in fetch(t):
                    cp.start()

        @pl.when(i + ahead < steps)
        def _():
            for cp in fetch(i + ahead):
                cp.start()

        for cp in fetch(i):
            cp.wait()
        acc = _dot(mix_buf[i % slots], wout_ref[...], 1, 0)
        for o_ref, val in zip(outs, epilogue(acc, x_buf[i % slots], *[r[...] for r in row_refs])):
            o_ref[...] = val.astype(o_ref.dtype)

    return pl.pallas_call(
        body, name="mm_out", grid=(steps,),
        in_specs=[HBM, pl.BlockSpec(wout.shape, lambda i: (0, 0)), HBM] + [_vec_spec(D)] * len(rows),
        out_specs=[_row_spec(D)] * 3, out_shape=[jax.ShapeDtypeStruct((S, D), dt) for dt in (F32, F32, BF16)],
        scratch_shapes=[pltpu.VMEM((slots, TR, D), mix.dtype), pltpu.VMEM((slots, TR, D), x.dtype),
                        pltpu.SemaphoreType.DMA((2, slots))],
        compiler_params=_cparams(),
    )(mix, wout, x, *rows)


def _pre_norm_bwd(dh, x, w, dx1):
    def body(dh_ref, x_ref, w_ref, dx1_ref, dx_ref, dw_ref):
        i = pl.program_id(0)
        dxa, dwt = _rms_bwd(x_ref[...], w_ref[...], dh_ref[...])
        dx_ref[...] = dx1_ref[...] + dxa

        @pl.when(i == 0)
        def _():
            dw_ref[...] = jnp.zeros_like(dw_ref)

        dw_ref[...] += jnp.sum(dwt, axis=0, keepdims=True)

    return pl.pallas_call(
        body, name="pre_norm_bwd", grid=(S // TR,),
        in_specs=[_row_spec(D), _row_spec(D), _vec_spec(D), _row_spec(D)], out_specs=[_row_spec(D), _vec_spec(D)],
        out_shape=[jax.ShapeDtypeStruct((S, D), F32), jax.ShapeDtypeStruct((1, D), F32)], compiler_params=_cparams(),
    )(dh, x, w, dx1)


BQ = 512
NQ = S // BQ
LANE_BETA, LANE_G = 8, 12


def _gate_lanes(shape):
    lane = lax.broadcasted_iota(jnp.int32, shape, 1)
    return lane < LANE_BETA, (lane >= LANE_BETA) & (lane < LANE_G), (lane >= LANE_G) & (lane < LANE_G + NGH)


def _gates(proj, bias_vec, alog_vec):
    def body(s_ref, b_ref, a_ref, o_ref, carry_ref):
        i = pl.program_id(0)

        @pl.when(i == 0)
        def _():
            carry_ref[...] = jnp.zeros_like(carry_ref)

        z = s_ref[...] + b_ref[...]
        tail = jnp.log(1.0 + jnp.exp(-jnp.abs(z)))
        sp = jnp.maximum(z, 0.0) + tail
        lf = jnp.minimum(z, 0.0) - tail
        r = lax.broadcasted_iota(jnp.int32, (BQ, BQ), 0)
        c = lax.broadcasted_iota(jnp.int32, (BQ, BQ), 1)
        tri = (c <= r).astype(F32)
        cum = _hdot(tri, lf) + carry_ref[...]
        carry_ref[...] = cum[BQ - 1:BQ, :]
        is_fox, is_beta, is_g = _gate_lanes(z.shape)
        o_ref[...] = jnp.where(is_fox, cum, jnp.where(is_beta, _sigmoid(z), jnp.where(is_g, -jnp.exp(a_ref[...]) * sp, 0.0)))

    return pl.pallas_call(
        body, name="gates", grid=(NQ,),
        in_specs=[pl.BlockSpec((BQ, LANES), lambda i: (i, BLK_SMALL)), _vec_spec(LANES), _vec_spec(LANES)],
        out_specs=pl.BlockSpec((BQ, LANES), lambda i: (i, 0)), out_shape=jax.ShapeDtypeStruct((S, LANES), F32),
        scratch_shapes=[pltpu.VMEM((1, LANES), F32)], compiler_params=_cparams(),
    )(proj, bias_vec, alog_vec)


def _gates_bwd(proj, bias_vec, alog_vec, dgates_gdn, dcum_fox, dproj):
    def body(s_ref, b_ref, a_ref, dg_ref, dc_ref, dproj_in, dproj_ref, red_ref, carry_ref):
        del dproj_in
        i = pl.program_id(0)

        @pl.when(i == 0)
        def _():
            carry_ref[...] = jnp.zeros_like(carry_ref)
            red_ref[...] = jnp.zeros_like(red_ref)

        z = s_ref[...] + b_ref[...]
        dg = dg_ref[...] + dc_ref[...]
        r = lax.broadcasted_iota(jnp.int32, (BQ, BQ), 0)
        c = lax.broadcasted_iota(jnp.int32, (BQ, BQ), 1)
        upper = (c >= r).astype(F32)
        dlf = _hdot(upper, dg) + carry_ref[...]
        carry_ref[...] = dlf[0:1, :]
        sig = _sigmoid(z)
        g_scale = -jnp.exp(a_ref[...])
        is_fox, is_beta, is_g = _gate_lanes(z.shape)
        ds = jnp.where(is_fox, dlf * (1.0 - sig), jnp.where(is_beta, dg * sig * (1.0 - sig), jnp.where(is_g, dg * g_scale * sig, 0.0)))
        dproj_ref[:, 0:LANES] = ds.astype(BF16)
        dproj_ref[:, LANES:2 * LANES] = jnp.zeros((BQ, LANES), BF16)
        dalog = jnp.where(is_g, dg * g_scale * _softplus(z), 0.0)
        sums = jnp.sum(ds, axis=0, keepdims=True)
        red_ref[0:1, :] += jnp.where(is_fox[0:1], sums, 0.0)
        red_ref[1:2, :] += pltpu.roll(jnp.where(is_g[0:1], sums, 0.0), LANES - LANE_G, 1)
        red_ref[2:3, :] += pltpu.roll(jnp.sum(dalog, axis=0, keepdims=True), LANES - LANE_G, 1)

    blk = pl.BlockSpec((BQ, LANES), lambda i: (NQ - 1 - i, 0))
    return pl.pallas_call(
        body, name="gates_bwd", grid=(NQ,),
        in_specs=[pl.BlockSpec((BQ, LANES), lambda i: (NQ - 1 - i, BLK_SMALL)), _vec_spec(LANES), _vec_spec(LANES), blk, blk,
                  pl.BlockSpec(memory_space=pl.ANY)],
        out_specs=[pl.BlockSpec((BQ, 2 * LANES), lambda i: (NQ - 1 - i, BLK_SMALL // 2)), pl.BlockSpec((8, LANES), lambda i: (0, 0))],
        out_shape=[jax.ShapeDtypeStruct((S, DPROJ_PAD), BF16), jax.ShapeDtypeStruct((8, LANES), F32)],
        input_output_aliases={5: 0},
        scratch_shapes=[pltpu.VMEM((1, LANES), F32)], compiler_params=_cparams(),
    )(proj, bias_vec, alog_vec, dgates_gdn, dcum_fox, dproj)


FOX_SCALE = FHD ** -0.5
FOX_PAIRS = 2
FOX_PAIRS_BWD = 2


def _head_mask(e):
    lane = lax.broadcasted_iota(jnp.int32, (1, LANES), 1)
    return (lane >= e * FHD) & (lane < (e + 1) * FHD)


def _lane_col(vals, index):
    lane = lax.broadcasted_iota(jnp.int32, vals.shape, 1)
    return jnp.sum(jnp.where(lane == index, vals, 0.0), axis=1, keepdims=True)


def _sublane_row(vals, index):
    row = lax.broadcasted_iota(jnp.int32, vals.shape, 0)
    return jnp.sum(jnp.where(row == index, vals, 0.0), axis=0, keepdims=True)


def _pair_cols(c0, c1):
    lane = lax.broadcasted_iota(jnp.int32, (c0.shape[0], 2), 1)
    return jnp.where(lane == 0, c0, c1)


def _split3(x):
    hi = x.astype(BF16).astype(F32)
    rest = x - hi
    mid = rest.astype(BF16).astype(F32)
    return hi, mid, (rest - mid).astype(BF16).astype(F32)


def _fox_operand(vals, e, cum, is_query):
    lane = lax.broadcasted_iota(jnp.int32, (1, LANES), 1)
    base = (1 - e) * FHD
    parts = _split3(cum)
    own = jnp.where(_head_mask(e), vals * FOX_SCALE if is_query else vals, 0.0)
    cum_at, ones_at = (base, base + 3) if is_query else (base + 3, base)
    sign = 1.0 if is_query else -1.0
    out = own + jnp.where((lane >= ones_at) & (lane < ones_at + 3), 1.0, 0.0)
    for i, part in enumerate(parts):
        out = out + jnp.where(lane == cum_at + i, sign * part, 0.0)
    return out.astype(BF16)


def _causal_block():
    return lax.broadcasted_iota(jnp.int32, (BQ, BQ), 1) <= lax.broadcasted_iota(jnp.int32, (BQ, BQ), 0)


def _head_rms(o, masks):
    o2 = o * o
    r = [lax.rsqrt(jnp.sum(jnp.where(mk, o2, 0.0), axis=1, keepdims=True) * (1.0 / FHD) + EPS) for mk in masks]
    return jnp.where(masks[0], r[0], r[1])


def _hosted(exchange):
    if exchange is None:
        return [], [], [], [], []
    return (exchange.inputs, [HBM] * len(exchange.inputs), [HBM] * len(exchange.out_shape), exchange.out_shape,
            exchange.sem_shapes())


def _fox_fwd(proj, gates, w2, exchange=None):
    ex_in, ex_in_specs, ex_out_specs, ex_out_shape, ex_scratch = _hosted(exchange)

    n_in = 3 * FOX_PAIRS + 2
    heads = [(pp, e) for pp in range(FOX_PAIRS) for e in range(2)]

    def body(*refs):
        qkv_refs, g_ref, w_ref = refs[:3 * FOX_PAIRS], refs[3 * FOX_PAIRS], refs[3 * FOX_PAIRS + 1]
        mix_ref, o_ref, lse_ref = refs[n_in + len(ex_in):n_in + 3 + len(ex_in)]
        ka_ref, vb_ref = refs[n_in + 3 + len(ex_in) + len(ex_out_shape):n_in + 5 + len(ex_in) + len(ex_out_shape)]
        ex_refs = refs[n_in:n_in + len(ex_in)] + refs[n_in + 3 + len(ex_in):n_in + 3 + len(ex_in) + len(ex_out_shape)] + refs[-2:]
        grp, qi = pl.program_id(0), pl.program_id(1)

        def head_index(pp, e):
            return 2 * (FOX_PAIRS * grp + pp) + e

        if exchange is not None:
            @pl.when((grp == 0) & (qi == 0))
            def _():
                exchange.start(*exchange.split(ex_refs))

        @pl.when(qi == 0)
        def _():
            gt = g_ref[...]
            for pp in range(FOX_PAIRS):
                kv = qkv_refs[3 * pp + 1][...]
                for e in range(2):
                    ka_ref[2 * pp + e] = _fox_operand(kv, e, _lane_col(gt, head_index(pp, e)), False)
                vb_ref[pp] = qkv_refs[3 * pp + 2][...].astype(BF16)

        masks = [_head_mask(0), _head_mask(1)]
        gt = g_ref[pl.ds(pl.multiple_of(qi * BQ, BQ), BQ), :]
        qs = [_fox_operand(qkv_refs[3 * pp][...], e, _lane_col(gt, head_index(pp, e)), True) for pp, e in heads]
        n = range(len(heads))

        def block(kj, carry, diagonal):
            rows = pl.ds(pl.multiple_of(kj * BQ, BQ), BQ)
            s = [_dot(qs[i], ka_ref[i, rows, :], 1, 1) for i in n]
            if diagonal:
                s = [jnp.where(_causal_block(), s[i], -jnp.inf) for i in n]
            m_new = [jnp.maximum(carry[i][0], jnp.max(s[i], axis=-1, keepdims=True)) for i in n]
            p = [jnp.exp(s[i] - m_new[i]) for i in n]
            alpha = [jnp.exp(carry[i][0] - m_new[i]) for i in n]
            l_new = [alpha[i] * carry[i][1] + jnp.sum(p[i], axis=-1, keepdims=True) for i in n]
            pv = [_dot(p[i], vb_ref[heads[i][0], rows, :]) for i in n]
            return tuple((m_new[i], l_new[i], alpha[i] * carry[i][2] + pv[i]) for i in n)

        one = (jnp.full((BQ, 1), -jnp.inf, F32), jnp.zeros((BQ, 1), F32), jnp.zeros((BQ, LANES), F32))
        below = lax.fori_loop(0, qi, lambda kj, carry: block(kj, carry, False), (one,) * len(heads))
        done = block(qi, below, True)
        for pp in range(FOX_PAIRS):
            (m0, l0, a0), (m1, l1, a1) = done[2 * pp], done[2 * pp + 1]
            o = jnp.where(masks[0], a0 / l0, a1 / l1)
            cols = slice(pp * LANES, (pp + 1) * LANES)
            o_ref[:, cols] = o
            mix_ref[:, cols] = (o * _head_rms(o, masks) * w_ref[...]).astype(BF16)
            lse_ref[pp] = _pair_cols(m0 + jnp.log(l0), m1 + jnp.log(l1))

        if exchange is not None:
            @pl.when((grp == NPAIR // FOX_PAIRS // 2) & (qi == 0))
            def _():
                exchange.middle(*exchange.split(ex_refs))

            @pl.when((grp == NPAIR // FOX_PAIRS - 1) & (qi == NQ - 1))
            def _():
                exchange.rest(*exchange.split(ex_refs))

    qkv_specs = []
    for pp in range(FOX_PAIRS):
        qkv_specs.append(pl.BlockSpec((BQ, LANES), lambda g, i, pp=pp: (i, 3 * (FOX_PAIRS * g + pp))))
        qkv_specs.append(pl.BlockSpec((S, LANES), lambda g, i, pp=pp: (0, 3 * (FOX_PAIRS * g + pp) + 1)))
        qkv_specs.append(pl.BlockSpec((S, LANES), lambda g, i, pp=pp: (0, 3 * (FOX_PAIRS * g + pp) + 2)))
    blk = pl.BlockSpec((BQ, FOX_PAIRS * LANES), lambda g, i: (i, g))
    res = pl.pallas_call(
        body, name="fox_fwd", grid=(NPAIR // FOX_PAIRS, NQ),
        in_specs=qkv_specs + [pl.BlockSpec((S, LANES), lambda g, i: (0, 0)), pl.BlockSpec((1, LANES), lambda g, i: (0, 0))]
        + ex_in_specs,
        out_specs=[blk, blk, pl.BlockSpec((FOX_PAIRS, BQ, 2), lambda g, i: (g, i, 0))] + ex_out_specs,
        out_shape=[jax.ShapeDtypeStruct((S, D), BF16), jax.ShapeDtypeStruct((S, DFOX), F32),
                   jax.ShapeDtypeStruct((NPAIR, S, 2), F32)] + ex_out_shape,
        scratch_shapes=[pltpu.VMEM((2 * FOX_PAIRS, S, LANES), BF16), pltpu.VMEM((FOX_PAIRS, S, LANES), BF16)] + ex_scratch,
        compiler_params=_cparams(),
    )(*([proj] * (3 * FOX_PAIRS)), gates, w2, *ex_in)
    return res[0], res[1], res[2], res[3:]


def _fox_norm_bwd(o, dmix, w2, exchange=None):
    ex_in, ex_in_specs, ex_out_specs, ex_out_shape, ex_scratch = _hosted(exchange)

    def body(*refs):
        o_ref, g_ref, w_ref = refs[:3]
        do_ref, dl_ref, dw_ref = refs[3 + len(ex_in):6 + len(ex_in)]
        ex_refs = refs[3:3 + len(ex_in)] + refs[6 + len(ex_in):]
        hp, qi = pl.program_id(0), pl.program_id(1)

        if exchange is not None:
            @pl.when((hp == 0) & (qi == 0))
            def _():
                exchange.start(*exchange.split(ex_refs))

        masks = [_head_mask(0), _head_mask(1)]
        ov = o_ref[...]
        g = g_ref[...]
        r = _head_rms(ov, masks)
        gw = g * w_ref[...]
        gwo = gw * ov
        mean = [jnp.sum(jnp.where(mk, gwo, 0.0), axis=1, keepdims=True) * (1.0 / FHD) for mk in masks]
        do = r * gw - ov * (r * r * r) * jnp.where(masks[0], mean[0], mean[1])
        do_ref[...] = do.astype(BF16)
        doo = do * ov
        dl_ref[...] = _pair_cols(*[jnp.sum(jnp.where(mk, doo, 0.0), axis=1, keepdims=True) for mk in masks])

        @pl.when((hp == 0) & (qi == 0))
        def _():
            dw_ref[...] = jnp.zeros_like(dw_ref)

        dw_ref[...] += jnp.sum(g * ov * r, axis=0, keepdims=True)

        @pl.when((hp == NPAIR - 1) & (qi == NQ - 1))
        def _():
            dw = dw_ref[...]
            dw_ref[...] = dw + pltpu.roll(dw, FHD, 1)
            if exchange is not None:
                exchange.finish(*exchange.split(ex_refs))

    blk = pl.BlockSpec((BQ, LANES), lambda hp, i: (i, hp))
    vec = pl.BlockSpec((1, LANES), lambda hp, i: (0, 0))
    res = pl.pallas_call(
        body, name="fox_norm_bwd", grid=(NPAIR, NQ), in_specs=[blk, blk, vec] + ex_in_specs,
        out_specs=[blk, pl.BlockSpec((None, BQ, 2), lambda hp, i: (hp, i, 0)), vec] + ex_out_specs,
        out_shape=[jax.ShapeDtypeStruct((S, DFOX), BF16), jax.ShapeDtypeStruct((NPAIR, S, 2), F32),
                   jax.ShapeDtypeStruct((1, LANES), F32)] + ex_out_shape,
        scratch_shapes=ex_scratch, compiler_params=_cparams(),
    )(o, dmix, w2, *ex_in)
    return res[0], res[1], res[2], res[3:]


def _fox_bwd(proj, do, gates, lse, delta, exchange=None):
    ex_in, ex_in_specs, ex_out_specs, ex_out_shape, ex_scratch = _hosted(exchange)

    pg = FOX_PAIRS_BWD
    n_in = 3 * pg + 4
    heads = [(pp, e) for pp in range(pg) for e in range(2)]

    def body(*refs):
        qkv_refs = refs[:3 * pg]
        do_ref, g_ref, lse_ref, dl_ref = refs[3 * pg:n_in]
        dproj_ref, dc_ref = refs[n_in + len(ex_in):n_in + 2 + len(ex_in)]
        qa_ref, dq_ref = refs[n_in + 2 + len(ex_in) + len(ex_out_shape):n_in + 4 + len(ex_in) + len(ex_out_shape)]
        ex_refs = refs[n_in:n_in + len(ex_in)] + refs[n_in + 2 + len(ex_in):n_in + 2 + len(ex_in) + len(ex_out_shape)] + refs[-2:]
        grp, kj = pl.program_id(0), pl.program_id(1)

        def head_index(pp, e):
            return 2 * (pg * grp + pp) + e

        if exchange is not None:
            @pl.when((grp == 0) & (kj == 0))
            def _():
                exchange.start(*exchange.split(ex_refs))

        @pl.when(kj == 0)
        def _():
            gt = g_ref[...]
            for pp in range(pg):
                qv = qkv_refs[3 * pp][...]
                for e in range(2):
                    qa_ref[2 * pp + e] = _fox_operand(qv, e, _lane_col(gt, head_index(pp, e)), True)
            dq_ref[...] = jnp.zeros_like(dq_ref)

        @pl.when((grp == 0) & (kj == 0))
        def _():
            dc_ref[...] = jnp.zeros_like(dc_ref)

        masks = [_head_mask(0), _head_mask(1)]
        krows = pl.ds(pl.multiple_of(kj * BQ, BQ), BQ)
        gk = g_ref[krows, :]
        kas = [_fox_operand(qkv_refs[3 * pp + 1][...], e, _lane_col(gk, head_index(pp, e)), False) for pp, e in heads]
        vbs = [qkv_refs[3 * pp + 2][...].astype(BF16) for pp in range(pg)]
        lane = lax.broadcasted_iota(jnp.int32, (BQ, LANES), 1)
        n = range(len(heads))

        def block(qi, carry, diagonal):
            dks, dvs, css = carry
            rows = pl.ds(pl.multiple_of(qi * BQ, BQ), BQ)
            qa = [qa_ref[i, rows, :] for i in n]
            s = [_dot(qa[i], kas[i], 1, 1) for i in n]
            if diagonal:
                s = [jnp.where(_causal_block(), s[i], -jnp.inf) for i in n]
            dov = [do_ref[rows, pp * LANES:(pp + 1) * LANES] for pp in range(pg)]
            doe = [jnp.where(masks[e], dov[pp], jnp.zeros_like(dov[pp])) for pp, e in heads]
            lse2 = [lse_ref[pp, rows, :] for pp in range(pg)]
            dl2 = [dl_ref[pp, rows, :] for pp in range(pg)]
            p = [jnp.exp(s[i] - _lane_col(lse2[heads[i][0]], heads[i][1])) for i in n]
            dp = [_dot(doe[i], vbs[heads[i][0]], 1, 1) for i in n]
            ds = [p[i] * (dp[i] - _lane_col(dl2[heads[i][0]], heads[i][1])) for i in n]
            dv_part = [_dot(p[i], doe[i], 0, 0) for i in n]
            dk_part = [_dot(ds[i], jnp.where(masks[heads[i][1]], qa[i], jnp.zeros_like(qa[i])), 0, 0) for i in n]
            dq_part = [jnp.where(masks[heads[i][1]], _dot(ds[i], kas[i]), 0.0) for i in n]
            css = tuple(css[i] + jnp.sum(ds[i], axis=0, keepdims=True) for i in n)
            dc = jnp.zeros((BQ, LANES), F32)
            for i in n:
                dc = dc + jnp.where(lane == head_index(*heads[i]), jnp.sum(ds[i], axis=1, keepdims=True), 0.0)
            for pp in range(pg):
                dq_ref[pp, rows, :] += (dq_part[2 * pp] + dq_part[2 * pp + 1]) * FOX_SCALE
            dc_ref[rows, :] += dc
            dks = tuple(dks[pp] + dk_part[2 * pp] + dk_part[2 * pp + 1] for pp in range(pg))
            dvs = tuple(dvs[pp] + dv_part[2 * pp] + dv_part[2 * pp + 1] for pp in range(pg))
            return dks, dvs, css

        zero = jnp.zeros((BQ, LANES), F32)
        first = block(kj, ((zero,) * pg, (zero,) * pg, (jnp.zeros((1, BQ), F32),) * len(heads)), True)
        dks, dvs, css = lax.fori_loop(kj + 1, NQ, lambda qi, carry: block(qi, carry, False), first)
        r = lax.broadcasted_iota(jnp.int32, (BQ, BQ), 0)
        c = lax.broadcasted_iota(jnp.int32, (BQ, BQ), 1)
        dcol = jnp.zeros((BQ, LANES), F32)
        for i in n:
            col = jnp.sum(jnp.where(r == c, css[i], 0.0), axis=1, keepdims=True)
            dcol = dcol + jnp.where(lane == head_index(*heads[i]), col, 0.0)
        dc_ref[krows, :] -= dcol
        for pp in range(pg):
            base = 3 * pp * LANES
            dproj_ref[krows, base + LANES:base + 2 * LANES] = dks[pp].astype(BF16)
            dproj_ref[krows, base + 2 * LANES:base + 3 * LANES] = dvs[pp].astype(BF16)

        @pl.when(kj == NQ - 1)
        def _():
            for pp in range(pg):
                dproj_ref[:, 3 * pp * LANES:(3 * pp + 1) * LANES] = dq_ref[pp].astype(BF16)

        if exchange is not None:
            @pl.when((grp == NPAIR // pg // 2) & (kj == 0))
            def _():
                exchange.middle(*exchange.split(ex_refs))

            @pl.when((grp == NPAIR // pg - 1) & (kj == NQ - 1))
            def _():
                exchange.rest(*exchange.split(ex_refs))

    qkv_specs = []
    for pp in range(pg):
        qkv_specs.append(pl.BlockSpec((S, LANES), lambda g, j, pp=pp: (0, 3 * (pg * g + pp))))
        qkv_specs.append(pl.BlockSpec((BQ, LANES), lambda g, j, pp=pp: (j, 3 * (pg * g + pp) + 1)))
        qkv_specs.append(pl.BlockSpec((BQ, LANES), lambda g, j, pp=pp: (j, 3 * (pg * g + pp) + 2)))
    pair = pl.BlockSpec((pg, S, 2), lambda g, j: (g, 0, 0))
    res = pl.pallas_call(
        body, name="fox_bwd", grid=(NPAIR // pg, NQ),
        in_specs=qkv_specs + [pl.BlockSpec((S, pg * LANES), lambda g, j: (0, g)), pl.BlockSpec((S, LANES), lambda g, j: (0, 0)),
                              pair, pair] + ex_in_specs,
        out_specs=[pl.BlockSpec((S, 3 * pg * LANES), lambda g, j: (0, g)), pl.BlockSpec((S, LANES), lambda g, j: (0, 0))]
        + ex_out_specs,
        out_shape=[jax.ShapeDtypeStruct((S, DPROJ_PAD), BF16), jax.ShapeDtypeStruct((S, LANES), F32)] + ex_out_shape,
        scratch_shapes=[pltpu.VMEM((2 * pg, S, LANES), BF16), pltpu.VMEM((pg, S, LANES), F32)] + ex_scratch,
        compiler_params=_cparams(),
    )(*([proj] * (3 * pg)), do, gates, lse, delta, *ex_in)
    return res[0], res[1], res[2:]


NQKV = 3 * NGH
GDN_QSCALE = GHD ** -0.5


def _shift_down(x, s):
    if s == 0:
        return x
    row = lax.broadcasted_iota(jnp.int32, x.shape, 0)
    return jnp.where(row >= s, pltpu.roll(x, s, 0), 0.0)


def _shift_up(x, s):
    if s == 0:
        return x
    n = x.shape[0]
    row = lax.broadcasted_iota(jnp.int32, x.shape, 0)
    return jnp.where(row < n - s, pltpu.roll(x, n - s, 0), 0.0)


def _conv_taps(xv):
    return [_shift_down(xv, CONV_K - 1 - j) for j in range(CONV_K)]


def _conv_pre(taps, wv):
    pre = taps[CONV_K - 1] * wv[CONV_K - 1:CONV_K, :]
    for j in range(CONV_K - 1):
        pre = pre + taps[j] * wv[j:j + 1, :]
    return pre


def _l2_factors(b):
    return b < 2 * NGH, jnp.where(b < NGH, GDN_QSCALE, 1.0)


def _gdn_pre(proj, conv_w):
    def body(x_ref, w_ref, o_ref):
        b = pl.program_id(0)
        c = _silu(_conv_pre(_conv_taps(x_ref[...]), w_ref[...]))
        normed, scale = _l2_factors(b)
        rs = lax.rsqrt(jnp.sum(c * c, axis=-1, keepdims=True) + EPS)
        o_ref[...] = c * jnp.where(normed, rs, 1.0) * scale

    return pl.pallas_call(
        body, name="gdn_pre", grid=(NQKV,),
        in_specs=[pl.BlockSpec((S, GHD), lambda b: (0, BLK_GDN + b)), pl.BlockSpec((CONV_K, GHD), lambda b: (0, b))],
        out_specs=pl.BlockSpec((S, GHD), lambda b: (0, b)),
        out_shape=jax.ShapeDtypeStruct((S, NQKV * GHD), F32), compiler_params=_cparams(),
    )(proj, conv_w)


def _gdn_pre_bwd(proj, conv_w, dqkv, dproj):
    def body(x_ref, w_ref, dy_ref, dproj_in, dx_ref, dw_ref):
        del dproj_in
        b = pl.program_id(0)
        taps = _conv_taps(x_ref[...])
        wv = w_ref[...]
        pre = _conv_pre(taps, wv)
        sig = _sigmoid(pre)
        c = pre * sig
        normed, scale = _l2_factors(b)
        g = dy_ref[...] * scale
        rs = lax.rsqrt(jnp.sum(c * c, axis=-1, keepdims=True) + EPS)
        dc_n = rs * g - c * (rs * rs * rs) * jnp.sum(g * c, axis=-1, keepdims=True)
        dc = jnp.where(normed, dc_n, g)
        dpre = dc * sig * (1.0 + pre * (1.0 - sig))
        dx = dpre * wv[CONV_K - 1:CONV_K, :]
        for j in range(CONV_K - 1):
            dx = dx + _shift_up(dpre, CONV_K - 1 - j) * wv[j:j + 1, :]
        dx_ref[...] = dx.astype(BF16)
        for j in range(CONV_K):
            dw_ref[j:j + 1, :] = jnp.sum(dpre * taps[j], axis=0, keepdims=True)

    return pl.pallas_call(
        body, name="gdn_pre_bwd", grid=(NQKV,),
        in_specs=[pl.BlockSpec((S, GHD), lambda b: (0, BLK_GDN + b)), pl.BlockSpec((CONV_K, GHD), lambda b: (0, b)),
                  pl.BlockSpec((None, S, GHD), lambda b: (b // NGH, 0, b % NGH)), pl.BlockSpec(memory_space=pl.ANY)],
        out_specs=[pl.BlockSpec((S, GHD), lambda b: (0, BLK_GDN + b)), pl.BlockSpec((CONV_K, GHD), lambda b: (0, b))],
        out_shape=[jax.ShapeDtypeStruct((S, DPROJ_PAD), BF16), jax.ShapeDtypeStruct((CONV_K, NQKV * GHD), F32)],
        input_output_aliases={3: 0}, compiler_params=_cparams(),
    )(proj, conv_w, dqkv, dproj)


CB = 16
NCB = NCH // CB


def _chunk_prep(qs, ks, vs, gcols, bcols, t_saved=None):
    n = range(len(qs))
    r = lax.broadcasted_iota(jnp.int32, (CHUNK, CHUNK), 0)
    c = lax.broadcasted_iota(jnp.int32, (CHUNK, CHUNK), 1)
    incl = c <= r
    eye = (r == c).astype(F32)
    grow = [jnp.sum(gcols[i] * eye, axis=0, keepdims=True) for i in n]
    gc_col = [jnp.sum(jnp.where(incl, grow[i], 0.0), axis=1, keepdims=True) for i in n]
    gc_row = [jnp.sum(jnp.where(r <= c, gcols[i], 0.0), axis=0, keepdims=True) for i in n]
    decay = [jnp.exp(jnp.where(incl, gc_col[i] - gc_row[i], -jnp.inf)) for i in n]
    kb = [ks[i] * bcols[i] for i in n]
    vb = [vs[i] * bcols[i] for i in n]
    kk = [_mm_nt(kb[i], ks[i]) for i in n]
    m = [jnp.where(c < r, kk[i] * decay[i], 0.0) for i in n]
    if t_saved is None:
        t_inv = [eye - m[i] for i in n]
        p = [_dot3(m[i], m[i]) for i in n]
        for step in range(5):
            t_inv = [t_inv[i] + _dot3(t_inv[i], p[i]) for i in n]
            if step < 4:
                p = [_dot3(p[i], p[i]) for i in n]
    else:
        t_inv = [_saved_inverse(m[i], t_saved[i]) for i in n]
    egc = [jnp.exp(gc_col[i]) for i in n]
    u = [_mm_nn(t_inv[i], vb[i]) for i in n]
    w = [_mm_nn(t_inv[i], kb[i] * egc[i]) for i in n]
    qk = [_mm_nt(qs[i], ks[i]) for i in n]
    gc_last = [gc_col[i][CHUNK - 1:CHUNK, :] for i in n]
    return [(u[i], w[i], qk[i] * decay[i], qs[i] * egc[i], ks[i] * jnp.exp(gc_last[i] - gc_col[i]), jnp.exp(gc_last[i]),
             t_inv[i]) for i in n]


def _prep_specs():
    rows = CB * CHUNK
    qs = pl.BlockSpec((rows, GHD), lambda i, h: (i, h))
    ks = pl.BlockSpec((rows, GHD), lambda i, h: (i, NGH + h))
    vs = pl.BlockSpec((rows, GHD), lambda i, h: (i, 2 * NGH + h))
    gs = pl.BlockSpec((rows, LANES), lambda i, h: (i, 0))
    a_s = pl.BlockSpec((None, rows, CHUNK), lambda i, h: (h, i, 0))
    gl_s = pl.BlockSpec((None, CB, 1, LANES), lambda i, h: (h, i, 0, 0))
    return qs, ks, vs, gs, a_s, gl_s


def _gdn_prep(qkv, gates, exchange=None):
    ex_in, ex_in_specs, ex_out_specs, ex_out_shape, ex_scratch = _hosted(exchange)

    def body(*refs):
        q_ref, k_ref, v_ref, g_ref = refs[:4]
        u_ref, w_ref, qd_ref, kd_ref, a_ref, gl_ref, t_ref = refs[4 + len(ex_in):11 + len(ex_in)]
        ex_refs = refs[4:4 + len(ex_in)] + refs[11 + len(ex_in):]
        h = pl.program_id(1)

        if exchange is not None:
            @pl.when((pl.program_id(0) == 0) & (h == 0))
            def _():
                exchange.start(*exchange.split(ex_refs))

        chunks = [pl.ds(cidx * CHUNK, CHUNK) for cidx in range(CB)]
        gts = [g_ref[rows, :] for rows in chunks]
        outs = _chunk_prep([q_ref[rows, :] for rows in chunks], [k_ref[rows, :] for rows in chunks],
                           [v_ref[rows, :] for rows in chunks], [_lane_col(gt, LANE_G + h) for gt in gts],
                           [_lane_col(gt, LANE_BETA + h) for gt in gts])
        for cidx, rows in enumerate(chunks):
            u, w, a, qd, kd, gl, t_inv = outs[cidx]
            u_ref[rows, :] = u
            w_ref[rows, :] = w
            qd_ref[rows, :] = qd
            kd_ref[rows, :] = kd
            a_ref[rows, :] = a
            t_ref[rows, :] = t_inv
            gl_ref[cidx] = jnp.broadcast_to(gl, (1, LANES))

        if exchange is not None:
            @pl.when((pl.program_id(0) == NCB // 2) & (h == 0))
            def _():
                exchange.middle(*exchange.split(ex_refs))

            @pl.when((pl.program_id(0) == NCB - 1) & (h == NGH - 1))
            def _():
                exchange.rest(*exchange.split(ex_refs))

    qs, ks, vs, gs, a_s, gl_s = _prep_specs()
    tok = jax.ShapeDtypeStruct((S, DGDN), F32)
    sq = jax.ShapeDtypeStruct((NGH, S, CHUNK), F32)
    res = pl.pallas_call(
        body, name="gdn_prep", grid=(NCB, NGH), in_specs=[qs, ks, vs, gs] + ex_in_specs,
        out_specs=[qs, qs, qs, qs, a_s, gl_s, a_s] + ex_out_specs,
        out_shape=[tok, tok, tok, tok, sq, jax.ShapeDtypeStruct((NGH, NCH, 1, LANES), F32), sq] + ex_out_shape,
        scratch_shapes=ex_scratch, compiler_params=_cparams(),
    )(qkv, qkv, qkv, gates, *ex_in)
    return res[:7], res[7:]


def _gdn_prep_bwd(qkv, gates, t_inv, du, dw, dqd, dkd, da, dgl, exchange=None):
    ex_in, ex_in_specs, ex_out_specs, ex_out_shape, ex_scratch = _hosted(exchange)

    def body(*refs):
        q_ref, k_ref, v_ref, g_ref, t_ref, du_ref, dw_ref, dqd_ref, dkd_ref, da_ref, dgl_ref = refs[:11]
        dqkv_ref, dg_ref = refs[11 + len(ex_in):13 + len(ex_in)]
        ex_refs = refs[11:11 + len(ex_in)] + refs[13 + len(ex_in):]
        h = pl.program_id(1)

        if exchange is not None:
            @pl.when((pl.program_id(0) == 0) & (h == 0))
            def _():
                exchange.start(*exchange.split(ex_refs))

        @pl.when(h == 0)
        def _():
            dg_ref[...] = jnp.zeros_like(dg_ref)

        lane = lax.broadcasted_iota(jnp.int32, (CHUNK, LANES), 1)
        chunks = [pl.ds(cidx * CHUNK, CHUNK) for cidx in range(CB)]
        gts = [g_ref[rows, :] for rows in chunks]
        t_saved = [t_ref[rows, :] for rows in chunks]
        _, vjp = jax.vjp(lambda *args: [o[:6] for o in _chunk_prep(*args, t_saved=t_saved)],
                         [q_ref[rows, :] for rows in chunks], [k_ref[rows, :] for rows in chunks],
                         [v_ref[rows, :] for rows in chunks], [_lane_col(gt, LANE_G + h) for gt in gts],
                         [_lane_col(gt, LANE_BETA + h) for gt in gts])
        dqs, dks, dvs, dgcs, dbcs = vjp([(du_ref[rows, :], dw_ref[rows, :], da_ref[rows, :], dqd_ref[rows, :],
                                          dkd_ref[rows, :], dgl_ref[cidx][:, 0:1]) for cidx, rows in enumerate(chunks)])
        for cidx, rows in enumerate(chunks):
            dq, dk, dv, dgc, dbc = dqs[cidx], dks[cidx], dvs[cidx], dgcs[cidx], dbcs[cidx]
            dqkv_ref[0, rows, :] = dq
            dqkv_ref[1, rows, :] = dk
            dqkv_ref[2, rows, :] = dv
            dg_ref[rows, :] += jnp.where(lane == LANE_G + h, dgc, 0.0) + jnp.where(lane == LANE_BETA + h, dbc, 0.0)

        if exchange is not None:
            @pl.when((pl.program_id(0) == NCB - 1) & (h == NGH - 1))
            def _():
                exchange.finish(*exchange.split(ex_refs))

    qs, ks, vs, gs, a_s, gl_s = _prep_specs()
    res = pl.pallas_call(
        body, name="gdn_prep_bwd", grid=(NCB, NGH), in_specs=[qs, ks, vs, gs, a_s, qs, qs, qs, qs, a_s, gl_s] + ex_in_specs,
        out_specs=[pl.BlockSpec((3, CB * CHUNK, GHD), lambda i, h: (0, i, h)), gs] + ex_out_specs,
        out_shape=[jax.ShapeDtypeStruct((3, S, DGDN), F32), jax.ShapeDtypeStruct((S, LANES), F32)] + ex_out_shape,
        scratch_shapes=ex_scratch, compiler_params=_cparams(),
    )(qkv, qkv, qkv, gates, t_inv, du, dw, dqd, dkd, da, dgl, *ex_in)
    return res[0], res[1], res[2:]


def _scan_specs(nh, parts, reverse):
    wide, rows, chunks = nh * GHD, S // parts, NCH // parts

    def part(p):
        return parts - 1 - p if reverse else p

    hs = pl.BlockSpec((rows, wide), lambda g, p: (part(p), g))
    a_s = pl.BlockSpec((nh, rows, CHUNK), lambda g, p: (g, part(p), 0))
    gl_s = pl.BlockSpec((nh, chunks, 1, LANES), lambda g, p: (g, part(p), 0, 0))
    st_s = pl.BlockSpec((nh, chunks, GHD, GHD), lambda g, p: (g, part(p), 0, 0))
    gz_s = pl.BlockSpec((rows, wide), lambda g, p: (part(p), BLK_GZ // nh + g))
    mix_s = pl.BlockSpec((rows, wide), lambda g, p: (part(p), NPAIR // nh + g))
    return hs, a_s, gl_s, st_s, gz_s, mix_s


def _head_cols(hh):
    return slice(hh * GHD, (hh + 1) * GHD)


SCAN_HEADS, SCAN_PARTS = 4, 2
SCAN_HEADS_BWD, SCAN_PARTS_BWD = 4, 4


def _gdn_scan(u, w, qd, kd, a, gl, proj, w_norm, mix):
    heads = range(SCAN_HEADS)

    def body(u_ref, w_ref, qd_ref, kd_ref, a_ref, gl_ref, z_ref, wn_ref, mix_in, mix_ref, o_ref, st_ref, carry_ref):
        del mix_in

        @pl.when(pl.program_id(1) == 0)
        def _():
            carry_ref[...] = jnp.zeros_like(carry_ref)

        def step(ci, states):
            rows = pl.ds(pl.multiple_of(ci * CHUNK, CHUNK), CHUNK)
            for hh in heads:
                st_ref[hh, ci] = states[hh]
            ws = [_dot(w_ref[rows, _head_cols(hh)], states[hh]) for hh in heads]
            qs = [_dot(qd_ref[rows, _head_cols(hh)], states[hh]) for hh in heads]
            vn = [u_ref[rows, _head_cols(hh)] - ws[hh] for hh in heads]
            av = [_dot(a_ref[hh, rows, :], vn[hh]) for hh in heads]
            kv = [_dot(kd_ref[rows, _head_cols(hh)], vn[hh], 0, 0) for hh in heads]
            for hh in heads:
                o_ref[rows, _head_cols(hh)] = qs[hh] + av[hh]
            return tuple(states[hh] * gl_ref[hh, ci] + kv[hh] for hh in heads)

        last = lax.fori_loop(0, NCH // SCAN_PARTS, step, tuple(carry_ref[hh] for hh in heads))
        for hh in heads:
            carry_ref[hh] = last[hh]
            ov = o_ref[:, _head_cols(hh)]
            mix_ref[:, _head_cols(hh)] = (ov * _rms_scale(ov) * wn_ref[...] * _silu(z_ref[:, _head_cols(hh)])).astype(BF16)

    hs, a_s, gl_s, st_s, gz_s, mix_s = _scan_specs(SCAN_HEADS, SCAN_PARTS, False)
    return pl.pallas_call(
        body, name="gdn_scan", grid=(NGH // SCAN_HEADS, SCAN_PARTS),
        in_specs=[hs, hs, hs, hs, a_s, gl_s, gz_s, pl.BlockSpec((1, GHD), lambda g, p: (0, 0)),
                  pl.BlockSpec(memory_space=pl.ANY)],
        out_specs=[mix_s, hs, st_s],
        out_shape=[jax.ShapeDtypeStruct((S, D), BF16), jax.ShapeDtypeStruct((S, DGDN), F32),
                   jax.ShapeDtypeStruct((NGH, NCH, GHD, GHD), F32)],
        input_output_aliases={8: 0}, scratch_shapes=[pltpu.VMEM((SCAN_HEADS, GHD, GHD), F32)], compiler_params=_cparams(),
    )(u, w, qd, kd, a, gl, proj, w_norm, mix)


def _gdn_scan_bwd(dmix, o, proj, w_norm, u, w, qd, kd, a, gl, states, dproj, exchange=None):
    ex_in, ex_in_specs, ex_out_specs, ex_out_shape, ex_scratch = _hosted(exchange)
    groups = NGH // SCAN_HEADS_BWD

    def body(*refs):
        dy_ref, o_ref, z_ref, wn_ref, u_ref, w_ref, qd_ref, kd_ref, a_ref, gl_ref, st_ref = refs[:11]
        dz_ref, du_ref, dw_ref, dqd_ref, dkd_ref, da_ref, dgl_ref, dwn_ref = refs[12 + len(ex_in):20 + len(ex_in)]
        do_ref, carry_ref = refs[20 + len(ex_in) + len(ex_out_shape):22 + len(ex_in) + len(ex_out_shape)]
        ex_refs = refs[12:12 + len(ex_in)] + refs[20 + len(ex_in):20 + len(ex_in) + len(ex_out_shape)] + refs[-2:]
        heads = range(SCAN_HEADS_BWD)
        chunks = NCH // SCAN_PARTS_BWD

        if exchange is not None:
            @pl.when((pl.program_id(0) == 0) & (pl.program_id(1) == 0))
            def _():
                exchange.start(*exchange.split(ex_refs))

        @pl.when((pl.program_id(0) == 0) & (pl.program_id(1) == 0))
        def _():
            dwn_ref[...] = jnp.zeros_like(dwn_ref)

        @pl.when(pl.program_id(1) == 0)
        def _():
            carry_ref[...] = jnp.zeros_like(carry_ref)

        wn = wn_ref[...]
        for hh in heads:
            c = _head_cols(hh)
            ov = o_ref[:, c]
            zv = z_ref[:, c]
            g = dy_ref[:, c]
            sig = _sigmoid(zv)
            dz_ref[:, c] = (g * (ov * _rms_scale(ov) * wn) * sig * (1.0 + zv * (1.0 - sig))).astype(BF16)
            do, dwt = _rms_bwd(ov, wn, g * zv * sig)
            do_ref[:, c] = do
            dwn_ref[...] += jnp.sum(dwt, axis=0, keepdims=True)

        def step(t, dstates):
            ci = chunks - 1 - t
            rows = pl.ds(pl.multiple_of(ci * CHUNK, CHUNK), CHUNK)
            cols = [_head_cols(hh) for hh in heads]
            state = [st_ref[hh, ci] for hh in heads]
            dov = [do_ref[rows, cols[hh]] for hh in heads]
            wv = [w_ref[rows, cols[hh]] for hh in heads]
            ws = [_dot(wv[hh], state[hh]) for hh in heads]
            adov = [_dot(a_ref[hh, rows, :], dov[hh], 0, 0) for hh in heads]
            kds = [_dot(kd_ref[rows, cols[hh]], dstates[hh]) for hh in heads]
            dqd = [_dot(dov[hh], state[hh], 1, 1) for hh in heads]
            qdo = [_dot(qd_ref[rows, cols[hh]], dov[hh], 0, 0) for hh in heads]
            vn = [u_ref[rows, cols[hh]] - ws[hh] for hh in heads]
            dvn = [adov[hh] + kds[hh] for hh in heads]
            da = [_dot(dov[hh], vn[hh], 1, 1) for hh in heads]
            dkd = [_dot(vn[hh], dstates[hh], 1, 1) for hh in heads]
            dwv = [_dot(dvn[hh], state[hh], 1, 1) for hh in heads]
            wdv = [_dot(wv[hh], dvn[hh], 0, 0) for hh in heads]
            for hh in heads:
                da_ref[hh, rows, :] = da[hh]
                dqd_ref[rows, cols[hh]] = dqd[hh]
                dkd_ref[rows, cols[hh]] = dkd[hh]
                dgl = jnp.sum(jnp.sum(dstates[hh] * state[hh], axis=1, keepdims=True), axis=0, keepdims=True)
                dgl_ref[hh, ci] = jnp.broadcast_to(dgl, (1, LANES))
                du_ref[rows, cols[hh]] = dvn[hh]
                dw_ref[rows, cols[hh]] = -dwv[hh]
            return tuple(dstates[hh] * gl_ref[hh, ci] + qdo[hh] - wdv[hh] for hh in heads)

        last = lax.fori_loop(0, chunks, step, tuple(carry_ref[hh] for hh in heads))
        for hh in heads:
            carry_ref[hh] = last[hh]

        if exchange is not None:
            @pl.when((pl.program_id(0) == groups - 1) & (pl.program_id(1) == SCAN_PARTS_BWD - 1))
            def _():
                exchange.finish(*exchange.split(ex_refs))

    hs, a_s, gl_s, st_s, gz_s, mix_s = _scan_specs(SCAN_HEADS_BWD, SCAN_PARTS_BWD, True)
    vec = pl.BlockSpec((1, GHD), lambda g, p: (0, 0))
    tok = jax.ShapeDtypeStruct((S, DGDN), F32)
    res = pl.pallas_call(
        body, name="gdn_scan_bwd", grid=(groups, SCAN_PARTS_BWD),
        in_specs=[mix_s, hs, gz_s, vec, hs, hs, hs, hs, a_s, gl_s, st_s, pl.BlockSpec(memory_space=pl.ANY)] + ex_in_specs,
        out_specs=[gz_s, hs, hs, hs, hs, a_s, gl_s, vec] + ex_out_specs,
        out_shape=[jax.ShapeDtypeStruct((S, DPROJ_PAD), BF16), tok, tok, tok, tok,
                   jax.ShapeDtypeStruct((NGH, S, CHUNK), F32), jax.ShapeDtypeStruct((NGH, NCH, 1, LANES), F32),
                   jax.ShapeDtypeStruct((1, GHD), F32)] + ex_out_shape,
        input_output_aliases={11: 0},
        scratch_shapes=[pltpu.VMEM((S // SCAN_PARTS_BWD, SCAN_HEADS_BWD * GHD), F32),
                        pltpu.VMEM((SCAN_HEADS_BWD, GHD, GHD), F32)] + ex_scratch,
        compiler_params=_cparams(),
    )(dmix, o, proj, w_norm, u, w, qd, kd, a, gl, states, dproj, *ex_in)
    return res[:8], res[8:]


def _place():
    return lax.axis_index("x"), lax.axis_index("y"), lax.axis_index("c")


def _place_scalars():
    x, y, c = _place()
    return jnp.stack([2 * x + y, c]).astype(jnp.int32)


def _other_chips(x, y):
    return [(1 - x, y), (x, 1 - y), (1 - x, 1 - y)]


HBM = pl.BlockSpec(memory_space=pltpu.HBM)
VMEM = pl.BlockSpec(memory_space=pltpu.VMEM)


def _half_rows(ref_or_rows, half):
    rows = ref_or_rows // 2
    return pl.ds(pl.multiple_of(half * rows, rows), rows)


class _Exchange:
    def __init__(self, inputs, out_shape, n_sems, start, finish=None, middle=None, rest=None):
        self.inputs, self.out_shape, self.n_sems, self.start = inputs, out_shape, n_sems, start
        if finish is None:
            def finish(*refs):
                middle(*refs)
                rest(*refs)
        self.finish = finish
        self.middle = middle if middle is not None else (lambda *refs: None)
        self.rest = rest if rest is not None else finish

    def sem_shapes(self):
        return [pltpu.SemaphoreType.DMA((self.n_sems,)), pltpu.SemaphoreType.DMA((self.n_sems,))]

    def split(self, refs):
        n_in, n_out = len(self.inputs), len(self.out_shape)
        return refs[:n_in], refs[n_in:n_in + n_out], refs[n_in + n_out], refs[n_in + n_out + 1]


def _allgather_exchange(shards, whole=(), sent_rows=None):
    n, nw = len(shards), len(whole)
    slots = 8
    sent_rows = [s.shape[0] for s in shards] if sent_rows is None else sent_rows

    def plan(c, src, outs, send_sems, recv_sems):
        x, y, _ = _place()
        via_x, via_y, diagonal = _other_chips(x, y)
        id_x, id_y, id_diagonal = [2 * chip[0] + chip[1] for chip in (via_x, via_y, diagonal)]
        me, sibling = (x, y, c), (x, y, 1 - c)

        def rows_of(a, half, quarter):
            total = src[a].shape[0]
            first = half * (total // 2) + (0 if quarter is None else quarter * (total // 4))
            size = min(total // 2 if quarter is None else total // 4, sent_rows[a] - first)
            assert size > 0 and size % BF16_ROWS == 0, (a, half, quarter, size)
            return pl.ds(first, size)

        def copy(a, k, chip_index, half, quarter, to, from_src=False):
            rows = rows_of(a, half, quarter)
            dst = outs[a].at[chip_index, rows]
            return pltpu.make_async_remote_copy(
                src_ref=src[a].at[rows] if from_src else dst, dst_ref=dst, send_sem=send_sems.at[slots * a + k],
                recv_sem=recv_sems.at[slots * a + k], device_id=to, device_id_type=MESH)

        def whole_copy(b, k, chip_index, to):
            return pltpu.make_async_remote_copy(
                src_ref=src[n + b], dst_ref=outs[n + b].at[chip_index], send_sem=send_sems.at[slots * n + 3 * b + k],
                recv_sem=recv_sems.at[slots * n + 3 * b + k], device_id=to, device_id_type=MESH)

        first, stages, last = [], [], []
        for a in range(n):
            first += [copy(a, 0, 2 * x + y, c, None, (*via_x, c), True), copy(a, 1, 2 * x + y, c, None, (*via_y, c), True)]
            stages.append([
                (copy(a, 0, id_x, c, None, me),
                 [copy(a, 2, id_x, c, 0, (*via_y, c)), copy(a, 4, id_x, c, None, sibling)]),
                (copy(a, 1, id_y, c, None, me),
                 [copy(a, 3, id_y, c, 1, (*via_x, c)), copy(a, 5, id_y, c, None, sibling)]),
                (copy(a, 2, id_diagonal, c, 0, me), [copy(a, 6, id_diagonal, c, 0, sibling)]),
                (copy(a, 3, id_diagonal, c, 1, me), [copy(a, 7, id_diagonal, c, 1, sibling)]),
            ])
            last += [copy(a, 4, id_x, 1 - c, None, me), copy(a, 5, id_y, 1 - c, None, me),
                     copy(a, 6, id_diagonal, 1 - c, 0, me), copy(a, 7, id_diagonal, 1 - c, 1, me)]
        for b in range(nw):
            for k, (chip, index) in enumerate(((via_x, id_x), (via_y, id_y), (diagonal, id_diagonal))):
                first.append(whole_copy(b, k, 2 * x + y, (*chip, c)))
                last.append(whole_copy(b, k, index, me))
        return first, stages, last

    def each_core(step):
        def run(*refs):
            for core in (0, 1):
                @pl.when(lax.axis_index("c") == core)
                def _(core=core):
                    step(core, *refs)

        return run

    def start(core, *refs):
        for cp in plan(core, *refs)[0]:
            cp.start()

    def pass_on(stages, which):
        for stage in which:
            for per_shard in stages:
                lands, onward = per_shard[stage]
                lands.wait_recv()
                for cp in onward:
                    cp.start()

    def middle(core, *refs):
        pass_on(plan(core, *refs)[1], (0, 1))

    def rest(core, *refs):
        first, stages, last = plan(core, *refs)
        pass_on(stages, (2, 3))
        for cp in last:
            cp.wait_recv()
        for cp in first + [cp for per_shard in stages for _, onward in per_shard for cp in onward]:
            cp.wait_send()

    out_shape = [jax.ShapeDtypeStruct((NCHIP,) + s.shape, s.dtype) for s in list(shards) + list(whole)]
    return _Exchange(list(shards) + list(whole), out_shape, slots * n + 3 * nw, each_core(start),
                     middle=each_core(middle), rest=each_core(rest))


def _with_own(gathered, own):
    x, y, _ = _place()
    return lax.dynamic_update_index_in_dim(gathered, own, 2 * x + y, axis=0)


def _simple_exchange(inputs, out_shape, copies_of):
    def start(*refs):
        for cp in copies_of(*refs):
            cp.start()

    def finish(*refs):
        for cp in copies_of(*refs):
            cp.wait()

    return _Exchange(list(inputs), out_shape, len(out_shape) * 3, start, finish)


def _pair_exchange(grads):
    def copies_of(src, outs, send_sems, recv_sems):
        x, y, c = _place()
        return [pltpu.make_async_remote_copy(
            src_ref=src[a].at[:, _half_rows(src[a].shape[1], 1 - c)], dst_ref=outs[a], send_sem=send_sems.at[a],
            recv_sem=recv_sems.at[a], device_id=(x, y, 1 - c), device_id_type=MESH) for a in range(len(src))]

    return _simple_exchange(
        grads, [jax.ShapeDtypeStruct((g.shape[0], g.shape[1] // 2, g.shape[2]), g.dtype) for g in grads], copies_of)


def _pair_sum(grads, theirs, name):
    n = len(grads)

    def body(place_ref, *refs):
        for a in range(n):
            refs[2 * n + a][...] = (refs[a][...].astype(F32) + refs[n + a][...].astype(F32)).astype(BF16)

    def specs(arrs):
        return [pl.BlockSpec((None,) + g.shape[1:], lambda j, place: (j, 0, 0)) for g in arrs]

    own_half = [pl.BlockSpec((None, g.shape[1] // 2, g.shape[2]), lambda j, place: (j, place[1], 0)) for g in grads]
    return pl.pallas_call(
        body, name=name, grid_spec=pltpu.PrefetchScalarGridSpec(
            num_scalar_prefetch=1, grid=(NCHIP,), in_specs=own_half + specs(theirs), out_specs=specs(theirs)),
        out_shape=[jax.ShapeDtypeStruct(g.shape, BF16) for g in theirs], compiler_params=_cparams(),
    )(_place_scalars(), *grads, *theirs)


def _chip_exchange(parts, rows_by_core=None):
    if rows_by_core is None:
        rows_by_core = [(p.shape[1],) * 2 for p in parts]

    def each_core(act):
        def run(src, outs, send_sems, recv_sems):
            x, y, c = _place()
            for core in (0, 1):
                @pl.when(c == core)
                def _(core=core):
                    for a in range(len(src)):
                        rows = pl.ds(0, rows_by_core[a][core])
                        for k, chip in enumerate(_other_chips(x, y)):
                            act(pltpu.make_async_remote_copy(
                                src_ref=src[a].at[2 * chip[0] + chip[1], rows], dst_ref=outs[a].at[k, rows],
                                send_sem=send_sems.at[3 * a + k], recv_sem=recv_sems.at[3 * a + k],
                                device_id=(*chip, core), device_id_type=MESH))

        return run

    return _Exchange(list(parts), [jax.ShapeDtypeStruct((NCHIP - 1,) + p.shape[1:], p.dtype) for p in parts],
                     3 * len(parts), each_core(lambda cp: cp.start()), each_core(lambda cp: cp.wait()))


def _chip_sum(parts, received, exchange=None):
    n = len(parts)
    steps = 4
    ex_in, ex_in_specs, ex_out_specs, ex_out_shape, ex_scratch = _hosted(exchange)
    n_ex = len(ex_in)

    def body(place_ref, *refs):
        mine, theirs = refs[2 * n + n_ex:3 * n + n_ex], refs[3 * n + n_ex:4 * n + n_ex]
        ex_refs = refs[2 * n:2 * n + n_ex] + refs[4 * n + n_ex:len(refs) - n - 2]
        tiles, send_sems, recv_sems = refs[len(refs) - n - 2:len(refs) - 2], refs[-2], refs[-1]
        step = pl.program_id(0)
        if exchange is not None:
            @pl.when(step == 0)
            def _():
                exchange.start(*exchange.split(ex_refs))

        x, y, c = _place()

        def share(a, i):
            rows = tiles[a].shape[1]
            return pltpu.make_async_remote_copy(
                src_ref=tiles[a].at[i], dst_ref=theirs[a].at[pl.ds(pl.multiple_of(i * rows, rows), rows)],
                send_sem=send_sems.at[a * steps + i], recv_sem=recv_sems.at[a * steps + i], device_id=(x, y, 1 - c),
                device_id_type=MESH)

        for a in range(n):
            own, r = refs[a], refs[n + a]
            total = ((own[...].astype(F32) + r[0].astype(F32)) + r[1].astype(F32)) + r[2].astype(F32)
            mine[a][...] = total
            tiles[a][step] = total
            share(a, step).start()

        @pl.when(step == steps - 1)
        def _():
            if exchange is not None:
                exchange.finish(*exchange.split(ex_refs))
            for a in range(n):
                for i in range(steps):
                    share(a, i).wait()

    own_specs = [pl.BlockSpec((None, g.shape[1] // steps, g.shape[2]), lambda i, place: (place[0], i, 0)) for g in parts]
    received_specs = [pl.BlockSpec((g.shape[0], g.shape[1] // steps, g.shape[2]), lambda i, place: (0, i, 0))
                      for g in received]
    out_specs = [pl.BlockSpec((g.shape[1] // steps, g.shape[2]), lambda i, place: (i, 0)) for g in parts]
    halves = [jax.ShapeDtypeStruct(g.shape[1:], F32) for g in parts]
    res = pl.pallas_call(
        body, name="grads_chip_sum", grid_spec=pltpu.PrefetchScalarGridSpec(
            num_scalar_prefetch=1, grid=(steps,), in_specs=own_specs + received_specs + ex_in_specs,
            out_specs=out_specs + [HBM] * n + ex_out_specs,
            scratch_shapes=ex_scratch + [pltpu.VMEM((steps, g.shape[1] // steps, g.shape[2]), F32) for g in parts]
            + [pltpu.SemaphoreType.DMA((n * steps,))] * 2),
        out_shape=halves * 2 + ex_out_shape, compiler_params=_cparams(),
    )(_place_scalars(), *parts, *received, *ex_in)
    return res[:n], res[n:2 * n], res[2 * n:]


def _adamw_math(w, g, m, v):
    nm = ADAM_B1 * m + (1.0 - ADAM_B1) * g
    nv = ADAM_B2 * v + (1.0 - ADAM_B2) * jnp.square(g)
    m_hat = nm / (1.0 - ADAM_B1 ** ADAM_STEP)
    v_hat = nv / (1.0 - ADAM_B2 ** ADAM_STEP)
    return -ADAM_LR * (m_hat / (jnp.sqrt(v_hat) + ADAM_EPS) + ADAM_WD * w), nm, nv


def _adamw_big(ws, g_mine, g_theirs, ms, vs):
    n = len(ws)
    steps = 8
    per_half = steps // 2

    def body(place_ref, *refs):
        outs = refs[5 * n:]
        own_half = (pl.program_id(0) // per_half) == place_ref[1]
        for a in range(n):
            g = jnp.where(own_half, refs[n + a][...], refs[2 * n + a][...])
            d, nm, nv = _adamw_math(refs[a][...], g, refs[3 * n + a][...], refs[4 * n + a][...])
            outs[a][...] = g
            outs[n + a][...] = d
            outs[2 * n + a][...] = nm
            outs[3 * n + a][...] = nv

    specs = [pl.BlockSpec((w.shape[0] // steps, w.shape[1]), lambda i, place: (i, 0)) for w in ws]

    def half_specs(halves, of_this_core):
        def tile(i, place):
            first = (place[1] if of_this_core else 1 - place[1]) * per_half
            return jnp.clip(i - first, 0, per_half - 1), 0

        return [pl.BlockSpec((g.shape[0] // per_half, g.shape[1]), tile) for g in halves]

    shapes = [jax.ShapeDtypeStruct(w.shape, F32) for w in ws]
    res = pl.pallas_call(
        body, name="adamw_big", grid_spec=pltpu.PrefetchScalarGridSpec(
            num_scalar_prefetch=1, grid=(steps,),
            in_specs=specs + half_specs(g_mine, True) + half_specs(g_theirs, False) + specs * 2, out_specs=specs * 4),
        out_shape=shapes * 4, compiler_params=_cparams(),
    )(_place_scalars(), *ws, *g_mine, *g_theirs, *ms, *vs)
    return res[:n], res[n:2 * n], res[2 * n:3 * n], res[3 * n:]


def _adamw_in(w, g_mine, g_theirs, m, v):
    half = D // 2

    def body(w_ref, gm_ref, gt_ref, m_ref, v_ref, g_out, d_out, nm_out, nv_out, g_ref):
        south = lax.axis_index("c") == 0
        g_ref[0:half, :] = jnp.where(south, gm_ref[...], gt_ref[...])
        g_ref[half:D, :] = jnp.where(south, gt_ref[...], gm_ref[...])
        g = g_ref[0:CW, :]
        d, nm, nv = _adamw_math(w_ref[...], g, m_ref[...], v_ref[...])
        g_out[...] = g
        d_out[...] = d
        nm_out[...] = nm
        nv_out[...] = nv

    cols = 2 * LANES
    spec = pl.BlockSpec((CW, cols), lambda i: (0, i))
    half_spec = pl.BlockSpec((half, cols), lambda i: (0, i))
    return pl.pallas_call(
        body, name="adamw_in", grid=(D // cols,), in_specs=[spec, half_spec, half_spec, spec, spec], out_specs=[spec] * 4,
        out_shape=[jax.ShapeDtypeStruct((CW, D), F32)] * 4, scratch_shapes=[pltpu.VMEM((D, cols), F32)],
        compiler_params=_cparams(),
    )(w, g_mine, g_theirs, m, v)


NORM_NAMES = ("pre_mix_norm", "post_mix_norm", "pre_mlp_norm", "post_mlp_norm")
SMALL_NAMES = NORM_NAMES + ("gdn_conv_w", "fox_f_bias", "gdn_dt_bias", "gdn_a_log", "fox_out_norm", "gdn_out_norm")
CONV_COLS = 3 * DGDN // NCHIP


def _small_gather(d_norms, d_conv, sums, d_fox_norm, d_gdn_norm, loss_row):
    n_arrays = 6
    n_remote = n_arrays * (NDEV - 1)

    def copies_of(src, outs, send_sems, recv_sems):
        x, y, c = _place()
        me = 4 * x + 2 * y + c

        def from_me(chip_index):
            cols = pl.ds(pl.multiple_of(chip_index * CONV_COLS, LANES), CONV_COLS)
            return [src[0], src[1].at[:, cols], src[2], src[3], src[4], src[5]]

        local = [pltpu.make_async_copy(s, outs[a].at[me], send_sems.at[n_remote + a]) for a, s in enumerate(from_me(2 * x + y))]
        remote = []
        for k in range(1, NDEV):
            px, py, pc = x ^ ((k >> 2) & 1), y ^ ((k >> 1) & 1), c ^ (k & 1)
            remote += [pltpu.make_async_remote_copy(
                src_ref=s, dst_ref=outs[a].at[me], send_sem=send_sems.at[n_arrays * (k - 1) + a],
                recv_sem=recv_sems.at[n_arrays * (k - 1) + a], device_id=(px, py, pc), device_id_type=MESH)
                for a, s in enumerate(from_me(2 * px + py))]
        return local + remote

    def start(*refs):
        for cp in copies_of(*refs):
            cp.start()

    def finish(*refs):
        for cp in copies_of(*refs):
            cp.wait()

    shapes = [(4, D), (CONV_K, CONV_COLS), (8, LANES), (1, LANES), (1, LANES), (1, LANES)]
    return _Exchange([d_norms, d_conv, sums, d_fox_norm, d_gdn_norm, loss_row],
                     [jax.ShapeDtypeStruct((NDEV,) + s, F32) for s in shapes], n_remote + n_arrays, start, finish)


def _small_adamw(gathered, ws, ms, vs):
    n = len(SMALL_NAMES)
    ng = len(gathered)

    def body(*refs):
        def total(buf):
            acc = buf[0]
            for i in range(1, NDEV):
                acc = acc + buf[i]
            return acc

        t_norms, t_conv, t_sums, t_fn, t_gn, t_loss = [total(r) for r in refs[:ng]]
        w_refs, m_refs, v_refs = refs[ng:ng + n], refs[ng + n:ng + 2 * n], refs[ng + 2 * n:ng + 3 * n]
        outs = refs[ng + 3 * n:]
        outs[4 * n][...] = t_loss
        grads = [t_norms[i:i + 1, :] for i in range(4)] + [
            t_conv, t_sums[0:1, 0:NFH], t_sums[1:2, 0:NGH], t_sums[2:3, 0:NGH], t_fn[:, 0:FHD], t_gn]
        for a in range(n):
            d, nm, nv = _adamw_math(w_refs[a][...], grads[a], m_refs[a][...], v_refs[a][...])
            outs[a][...] = grads[a]
            outs[n + a][...] = d
            outs[2 * n + a][...] = nm
            outs[3 * n + a][...] = nv

    def whole(arr):
        return pl.BlockSpec(arr.shape, lambda i: (0,) * arr.ndim)

    res = pl.pallas_call(
        body, name="small_adamw", grid=(1,), in_specs=[whole(t) for t in gathered] + [whole(w) for w in ws] * 3,
        out_specs=[whole(w) for w in ws] * 4 + [pl.BlockSpec((1, LANES), lambda i: (0, 0))],
        out_shape=[jax.ShapeDtypeStruct(w.shape, F32) for w in ws] * 4 + [jax.ShapeDtypeStruct((1, LANES), F32)],
        compiler_params=_cparams(),
    )(*gathered, *ws, *ms, *vs)
    return res[:n], res[n:2 * n], res[2 * n:3 * n], res[3 * n:4 * n], res[4 * n]


CW = DPROJ // NCHIP
BF16_ROWS = 16
CW_SENT = -(-CW // BF16_ROWS) * BF16_ROWS
PROJ_RUNS = tuple((part * DFOX + hp * LANES, part * DFOX + (hp + 1) * LANES, (3 * hp + part) * LANES)
                  for hp in range(NPAIR) for part in range(3)) + (
    (1536, 1544, BLK_SMALL * LANES), (1544, 3080, BLK_GDN * LANES), (3080, 3088, BLK_SMALL * LANES + 8),
    (3088, 3600, BLK_GZ * LANES))


def _proj_pieces():
    pieces = []
    for lo, hi, at in PROJ_RUNS:
        while lo < hi:
            j = lo // CW
            end = min(hi, (j + 1) * CW)
            pieces.append((j, lo - j * CW, at, end - lo))
            at, lo = at + end - lo, end
    return pieces


RT = 256


def _to_padded_rows(gathered):
    def body(src_ref, out_ref, blocks_ref, rows_ref):
        blocks_ref[...] = src_ref[...].astype(F32)
        rows_ref[...] = jnp.zeros_like(rows_ref)
        for j, start, at, n in _proj_pieces():
            rows_ref[at:at + n, :] = blocks_ref[j, start:start + n, :]
        out_ref[...] = rows_ref[...].astype(out_ref.dtype)

    return pl.pallas_call(
        body, name="proj_rows_in", grid=(D // RT,), in_specs=[pl.BlockSpec((NCHIP, D, RT), lambda i: (0, 0, i))],
        out_specs=pl.BlockSpec((DPROJ_PAD, RT), lambda i: (0, i)), out_shape=jax.ShapeDtypeStruct((DPROJ_PAD, D), gathered.dtype),
        scratch_shapes=[pltpu.VMEM((NCHIP, D, RT), F32), pltpu.VMEM((DPROJ_PAD, RT), F32)], compiler_params=_cparams(),
    )(gathered)


def _from_padded_rows_pair_sum(w):
    steps = D // RT
    half = D // 2

    def body(src_ref, out_ref, rows_ref, blocks_ref, mine_ref, send_ref, recv_ref, send_sems, recv_sems):
        i = pl.program_id(0)
        x, y, c = _place()

        def share(t):
            return pltpu.make_async_remote_copy(
                src_ref=send_ref.at[t], dst_ref=recv_ref.at[t], send_sem=send_sems.at[t], recv_sem=recv_sems.at[t],
                device_id=(x, y, 1 - c), device_id_type=MESH)

        def rows_of(core):
            return blocks_ref[:, pl.ds(pl.multiple_of(core * half, half), half), :]

        @pl.when(i < steps)
        def _():
            rows_ref[...] = src_ref[...].astype(F32)
            blocks_ref[...] = jnp.zeros_like(blocks_ref)
            for j, start, at, n in _proj_pieces():
                blocks_ref[j, start:start + n, :] = rows_ref[at:at + n, :]
            mine_ref[i % 2] = rows_of(c)
            send_ref[i] = rows_of(1 - c).astype(BF16)
            share(i).start()

        @pl.when(i > 0)
        def _():
            share(i - 1).wait_recv()
            out_ref[...] = (mine_ref[(i - 1) % 2] + recv_ref[i - 1].astype(F32)).astype(BF16)

        @pl.when(i == steps)
        def _():
            for t in range(steps):
                share(t).wait_send()

    tile = (NCHIP, half, RT)
    return pl.pallas_call(
        body, name="proj_rows_out_pair_sum", grid=(steps + 1,),
        in_specs=[pl.BlockSpec((DPROJ_PAD, RT), lambda i: (0, jnp.minimum(i, steps - 1)))],
        out_specs=pl.BlockSpec(tile, lambda i: (0, 0, jnp.maximum(i - 1, 0))),
        out_shape=jax.ShapeDtypeStruct((NCHIP, half, D), BF16),
        scratch_shapes=[pltpu.VMEM((DPROJ_PAD, RT), F32), pltpu.VMEM((NCHIP, D, RT), F32), pltpu.VMEM((2,) + tile, F32),
                        pltpu.VMEM((steps,) + tile, BF16), pltpu.VMEM((steps,) + tile, BF16),
                        pltpu.SemaphoreType.DMA((steps,)), pltpu.SemaphoreType.DMA((steps,))],
        compiler_params=_cparams(),
    )(w)


def _local_step(x, target, first_weights, late_weights, reduce_late, reduce_in, pre_mix_norm, fox_f_bias, fox_out_norm,
                gdn_a_log, gdn_dt_bias, gdn_out_norm, post_mix_norm, pre_mlp_norm, post_mlp_norm):
    bias_vec = jnp.zeros((1, LANES), F32).at[0, 0:NFH].set(fox_f_bias).at[0, LANE_G:LANE_G + NGH].set(gdn_dt_bias)
    alog_vec = jnp.zeros((1, LANES), F32).at[0, LANE_G:LANE_G + NGH].set(gdn_a_log)
    w2 = jnp.concatenate([fox_out_norm, fox_out_norm], axis=1)

    h, first = _pre_norm(x, pre_mix_norm, exchange=first_weights[0])
    win_p, conv_w = first_weights[1](first)
    proj = _matmul(h, win_p, tb=True, tm=2048, tn=768, tk=1024, name="mm_proj", exchange=late_weights[0])
    proj, late_a = proj if late_weights[0] is not None else (proj, [])
    gates = _gates(proj, bias_vec, alog_vec)
    mix, fox_o, lse, late_b = _fox_fwd(proj, gates, w2, exchange=late_weights[1])
    qkv = _gdn_pre(proj, conv_w)
    (u, w, qd, kd, a_intra, gl, t_inv), _ = _gdn_prep(qkv, gates)
    wout, wup3 = late_weights[3](late_a, late_b)
    mix, gdn_raw, states = _gdn_scan(u, w, qd, kd, a_intra, gl, proj, gdn_out_norm, mix)
    def post_mix(acc, xv, w_post, w_pre_mlp):
        x1v = xv + acc * _rms_scale(acc) * w_post
        return acc, x1v, x1v * _rms_scale(x1v) * w_pre_mlp

    mixed, x1, h2 = _mm_out(mix, wout, x, post_mix, post_mix_norm, pre_mlp_norm)

    def relu2(acc):
        r = jnp.maximum(acc, 0.0)
        return r, r * r

    up_act = _matmul(h2, wup3, b3=True, tm=1024, tn=1024, tk=1024, out_dtypes=(BF16, BF16), epilogue=relu2,
                     name="mm_up", exchange=late_weights[2])
    (up_relu, act), late_c = up_act if late_weights[2] is not None else (up_act, [])
    wdown = late_weights[4](late_c)
    def loss_head(acc, x1v, tv, w):
        err = x1v + acc * _rms_scale(acc) * w - tv
        dx2v = err * (1.0 / D)
        dyv, dwt = _rms_bwd(acc, w, dx2v)
        part = 0.5 * jnp.sum(jnp.mean(err * err, axis=-1, keepdims=True), axis=0, keepdims=True)
        return dx2v, dyv, jnp.sum(dwt, axis=0, keepdims=True), jnp.broadcast_to(part, (1, D))

    dx2, dy, d_post_mlp, loss_wide = _matmul(
        act, wdown, tm=512, tn=D, tk=DFF, out_dtypes=(F32, BF16, F32, F32), extra=(x1, target, post_mlp_norm),
        epilogue=loss_head, n_sums=2, name="mm_down")
    loss_row = loss_wide[:, :LANES]

    dwdown = _matmul(act, dy, ta=True, tm=1024, tn=1024, tk=2048, out_dtypes=(BF16,), name="mm_dwdown")

    def relu2_bwd(acc, r):
        return (acc * 2.0 * r.astype(F32),)

    dup = _matmul(dy, wdown, tb=True, tm=1024, tn=1024, tk=1024, out_dtypes=(BF16,), extra=(up_relu,), epilogue=relu2_bwd,
                  name="mm_dact")
    dwup3 = _matmul(h2, dup, ta=True, tm=1024, tn=1024, tk=2048, out_dtypes=(BF16,), o3=True, name="mm_dwup")
    def mid_bwd(acc, x1v, dx2v, mixedv, w_pre_mlp, w_post):
        dxa, dwm = _rms_bwd(x1v, w_pre_mlp, acc)
        dx1v = dx2v + dxa
        dm, dwp = _rms_bwd(mixedv, w_post, dx1v)
        return dx1v, dm, jnp.sum(dwm, axis=0, keepdims=True), jnp.sum(dwp, axis=0, keepdims=True)

    dx1, dmixed, d_pre_mlp, d_post_mix = _matmul(
        dup, wup3, tb=True, b3=True, tm=512, tn=D, tk=DFF, out_dtypes=(F32, BF16, F32, F32),
        extra=(x1, dx2, mixed, pre_mlp_norm, post_mix_norm), epilogue=mid_bwd, n_sums=2, name="mm_dh2")
    dwout = _matmul(mix, dmixed, ta=True, tm=256, tn=1024, tk=2048, out_dtypes=(BF16,), name="mm_dwout")
    dmix = _matmul(dmixed, wout, tb=True, tm=512, tn=1024, tk=1024, name="mm_dmix")

    dfox, delta, d_fox_norm, from_sibling = _fox_norm_bwd(fox_o, dmix, w2, exchange=reduce_late[0](dwout, dwup3, dwdown))
    dproj, dcum_fox, reduced_a = _fox_bwd(proj, dfox, gates, lse, delta, exchange=reduce_late[1](from_sibling))
    (dproj, du, dw, dqd, dkd, da, dgl, d_gdn_norm), reduced_b = _gdn_scan_bwd(
        dmix, gdn_raw, proj, gdn_out_norm, u, w, qd, kd, a_intra, gl, states, dproj, exchange=reduce_late[2]())
    dqkv, dgates_gdn, reduced_c = _gdn_prep_bwd(qkv, gates, t_inv, du, dw, dqd, dkd, da, dgl, exchange=reduce_late[3]())
    reduced_late = (reduced_a, reduced_b, reduced_c)
    dproj, d_conv = _gdn_pre_bwd(proj, conv_w, dqkv, dproj)
    dproj, sums = _gates_bwd(proj, bias_vec, alog_vec, dgates_gdn, dcum_fox, dproj)

    dwin_p = _matmul(dproj, h, ta=True, tm=1280, tn=1024, tk=2048, out_dtypes=(BF16,), name="mm_dwin")
    exchange_in = reduce_in(dwin_p)
    dh = _matmul(dproj, win_p, tm=1024, tk=DPROJ_PAD, name="mm_dh", exchange=exchange_in)
    dh, reduced_in = dh if exchange_in is not None else (dh, [])
    grad_x, d_pre_mix = _pre_norm_bwd(dh, x, pre_mix_norm, dx1)

    d_norms = jnp.concatenate([d_pre_mix, d_post_mix, d_pre_mlp, d_post_mlp], axis=0)
    return grad_x, (d_norms, d_conv, sums, d_fox_norm, d_gdn_norm, loss_row), reduced_late, reduced_in


def kernel(x, pre_mix_norm, w_in, fox_f_bias, fox_out_norm, gdn_conv_w, gdn_a_log, gdn_dt_bias, gdn_out_norm, w_out, post_mix_norm, pre_mlp_norm, w_up, w_down, post_mlp_norm, loss_target, m_pre_mix_norm, m_w_in, m_fox_f_bias, m_fox_out_norm, m_gdn_conv_w, m_gdn_a_log, m_gdn_dt_bias, m_gdn_out_norm, m_w_out, m_post_mix_norm, m_pre_mlp_norm, m_w_up, m_w_down, m_post_mlp_norm, v_pre_mix_norm, v_w_in, v_fox_f_bias, v_fox_out_norm, v_gdn_conv_w, v_gdn_a_log, v_gdn_dt_bias, v_gdn_out_norm, v_w_out, v_post_mix_norm, v_pre_mlp_norm, v_w_up, v_w_down, v_post_mlp_norm):
    weights = dict(pre_mix_norm=pre_mix_norm, w_in=w_in, fox_f_bias=fox_f_bias, fox_out_norm=fox_out_norm, gdn_conv_w=gdn_conv_w,
                   gdn_a_log=gdn_a_log, gdn_dt_bias=gdn_dt_bias, gdn_out_norm=gdn_out_norm, w_out=w_out, post_mix_norm=post_mix_norm,
                   pre_mlp_norm=pre_mlp_norm, w_up=w_up, w_down=w_down, post_mlp_norm=post_mlp_norm)
    m_in = dict(pre_mix_norm=m_pre_mix_norm, w_in=m_w_in, fox_f_bias=m_fox_f_bias, fox_out_norm=m_fox_out_norm, gdn_conv_w=m_gdn_conv_w,
                gdn_a_log=m_gdn_a_log, gdn_dt_bias=m_gdn_dt_bias, gdn_out_norm=m_gdn_out_norm, w_out=m_w_out, post_mix_norm=m_post_mix_norm,
                pre_mlp_norm=m_pre_mlp_norm, w_up=m_w_up, w_down=m_w_down, post_mlp_norm=m_post_mlp_norm)
    v_in = dict(pre_mix_norm=v_pre_mix_norm, w_in=v_w_in, fox_f_bias=v_fox_f_bias, fox_out_norm=v_fox_out_norm, gdn_conv_w=v_gdn_conv_w,
                gdn_a_log=v_gdn_a_log, gdn_dt_bias=v_gdn_dt_bias, gdn_out_norm=v_gdn_out_norm, w_out=v_w_out, post_mix_norm=v_post_mix_norm,
                pre_mlp_norm=v_pre_mlp_norm, w_up=v_w_up, w_down=v_w_down, post_mlp_norm=v_post_mlp_norm)
    order_w = ("pre_mix_norm", "w_in", "fox_f_bias", "fox_out_norm", "gdn_conv_w", "gdn_a_log", "gdn_dt_bias", "gdn_out_norm", "w_out",
               "post_mix_norm", "pre_mlp_norm", "w_up", "w_down", "post_mlp_norm")
    big = ("w_in", "w_out", "w_up", "w_down")

    def row(v):
        return v if v.ndim == 2 else v.reshape(1, -1)

    win_shard = jnp.pad(w_in.T.astype(BF16), ((0, D - CW), (0, 0)))

    def resolve_first(gathered):
        win_g, conv_g = gathered
        return (_to_padded_rows(_with_own(win_g, win_shard)),
                _with_own(conv_g, gdn_conv_w).transpose(1, 0, 2).reshape(CONV_K, 3 * DGDN))

    late_shards = [weights[n].astype(BF16) for n in big[1:]]

    gathered_down = []

    def resolve_out_up(gathered_out, gathered_mlp):
        gathered_down.append(gathered_mlp[1])
        return _with_own(gathered_out[0], late_shards[0]).reshape(D, D), _with_own(gathered_mlp[0], late_shards[1])

    def resolve_down(_):
        return _with_own(gathered_down[0], late_shards[2]).reshape(DFF, D)

    pair_sums, late_blocks = {}, []

    def pair_summed(names, blocks, theirs):
        for n, s in zip(names, _pair_sum(blocks, theirs, "grads_pair_sum_" + names[0])):
            pair_sums[n] = s

    def late_pair_exchange(dwout, dwup3, dwdown):
        late_blocks.extend([dwout.reshape(NCHIP, D // NCHIP, D), dwup3, dwdown.reshape(NCHIP, DFF // NCHIP, D)])
        return _pair_exchange(late_blocks)

    def late_chip_exchange(theirs):
        pair_summed(big[1:], late_blocks, theirs)
        return _chip_exchange([pair_sums["w_up"], pair_sums["w_down"]])

    def reduce_in(dwin_p):
        pair_sums["w_in"] = _from_padded_rows_pair_sum(dwin_p)
        return _chip_exchange([pair_sums["w_in"]], rows_by_core=[(D // 2, CW_SENT - D // 2)])

    grad_x, small, received_late, received_in = _local_step(
        x[0], loss_target[0], (_allgather_exchange([win_shard], whole=[gdn_conv_w], sent_rows=[CW_SENT]), resolve_first),
        (_allgather_exchange(late_shards[:1]), _allgather_exchange(late_shards[1:]), None, resolve_out_up, resolve_down),
        (late_pair_exchange, late_chip_exchange, lambda: None, lambda: _chip_exchange([pair_sums["w_out"]])),
        reduce_in, row(pre_mix_norm), fox_f_bias, row(fox_out_norm), gdn_a_log, gdn_dt_bias,
        row(gdn_out_norm), row(post_mix_norm), row(pre_mlp_norm), row(post_mlp_norm))
    received_mlp, _, received_out = received_late

    g_mine, g_theirs, small_gathered = _chip_sum(
        [pair_sums[n] for n in big], list(received_in[:1]) + list(received_out[:1]) + list(received_mlp[:2]),
        exchange=_small_gather(*small))

    g_big, d_big, nm_big, nv_big = _adamw_big(
        [weights[n] for n in big[1:]], g_mine[1:], g_theirs[1:], [m_in[n] for n in big[1:]], [v_in[n] for n in big[1:]])
    in_t = _adamw_in(w_in.T, g_mine[0], g_theirs[0], m_w_in.T, v_w_in.T)
    g_small, d_small, nm_small, nv_small, loss_total = _small_adamw(
        small_gathered, [row(weights[n]) for n in SMALL_NAMES], [row(m_in[n]) for n in SMALL_NAMES],
        [row(v_in[n]) for n in SMALL_NAMES])

    grads, delta, new_m, new_v = {}, {}, {}, {}
    grads["w_in"], delta["w_in"], new_m["w_in"], new_v["w_in"] = [t.T for t in in_t]
    for i, n in enumerate(big[1:]):
        grads[n], delta[n], new_m[n], new_v[n] = g_big[i], d_big[i], nm_big[i], nv_big[i]
    for i, n in enumerate(SMALL_NAMES):
        shape = weights[n].shape
        grads[n], delta[n], new_m[n], new_v[n] = (g_small[i].reshape(shape), d_small[i].reshape(shape),
                                                  nm_small[i].reshape(shape), nv_small[i].reshape(shape))
    return (loss_total[0, 0], grad_x[None], *[grads[n] for n in order_w], *[delta[n] for n in order_w], *[new_m[n] for n in order_w],
            *[new_v[n] for n in order_w])
```
